```python
import jax, jax.numpy as jnp
from jax import lax
import numpy as np

D_MODEL = 1024
BATCH = 8
SEQ = 2048
DEPTH = 1

GRID_W = 64
CTX_LEN = 256
N_MOD = 6
HG_HEADS = 4
HG_DK = 128
HG_DV = 128
GLA_HEADS = 4
GLA_DK = 128
GLA_DV = 128
GLA_RANK = 16
GLA_GATE_NORM = 16.0
HG_W = HG_HEADS * HG_DK
HG_VW = HG_HEADS * HG_DV
GLA_KW = GLA_HEADS * GLA_DK
GLA_VW = GLA_HEADS * GLA_DV
D_FF = ((8 * D_MODEL // 3 + 255) // 256) * 256
EPS = 1e-6
IN_SPLITS = (HG_W, HG_VW, HG_W, HG_W, HG_VW, GLA_KW, GLA_KW, GLA_VW, GLA_VW, GLA_RANK, GLA_RANK, D_MODEL, D_MODEL)
IN_WIDTH = sum(IN_SPLITS)
IN_OFFSETS = tuple(int(v) for v in np.cumsum(IN_SPLITS)[:-1])

kernel_name = "hybrid_hgrn2_gla_prefix_dit_block"


def rms_norm(a, w):
    af = a.astype(jnp.float32)
    return (af * lax.rsqrt(jnp.mean(af * af, axis=-1, keepdims=True) + EPS)).astype(a.dtype) * w


def modulate(h, shift_c, scale_c, shift_x, scale_x):
    hc, hx = h[:, :CTX_LEN], h[:, CTX_LEN:]
    return jnp.concatenate([hc * (1 + scale_c) + shift_c,
                            hx * (1 + scale_x[:, None]) + shift_x[:, None]], axis=1)


def apply_gate(h, gate_c, gate_x):
    return jnp.concatenate([h[:, :CTX_LEN] * gate_c, h[:, CTX_LEN:] * gate_x[:, None]], axis=1)


def to_heads(a, n_heads):
    b, t, w = a.shape
    return a.reshape(b, t, n_heads, w // n_heads).transpose(0, 2, 1, 3)


def merge_heads(a):
    b, h, t, d = a.shape
    return a.transpose(0, 2, 1, 3).reshape(b, t, h * d)


def segment_reverse(a):
    return jnp.concatenate([jnp.flip(a[:, :, :CTX_LEN], axis=2), jnp.flip(a[:, :, CTX_LEN:], axis=2)], axis=2)


def chunk_scan(q, k, v, log_f, n_chunks):
    b, h, t, dk = q.shape
    dv = v.shape[-1]

    def chunks(a):
        return jnp.moveaxis(a.reshape(b, h, n_chunks, GRID_W, a.shape[-1]), 2, 0)

    causal = jnp.tril(jnp.ones((GRID_W, GRID_W), dtype=bool))[:, :, None]

    def step(s, blk):
        qc, kc, vc, gc = blk
        cum = jnp.cumsum(gc.astype(jnp.float32), axis=2)
        pair = jnp.exp(jnp.where(causal, cum[:, :, :, None, :] - cum[:, :, None, :, :], -jnp.inf))
        scores = jnp.einsum('bhtk,bhtsk,bhsk->bhts', qc, pair, kc)
        o = (jnp.einsum('bhts,bhsv->bhtv', scores, vc)
             + jnp.einsum('bhtk,bhkv->bhtv', qc * jnp.exp(cum), s))
        tail = jnp.exp(cum[:, :, -1:, :] - cum)
        s_new = (s * jnp.exp(cum[:, :, -1, :])[..., None]
                 + jnp.einsum('bhsk,bhsv->bhkv', kc * tail, vc))
        return s_new, o

    s0 = jnp.zeros((b, h, dk, dv), jnp.float32)
    _, o = lax.scan(step, s0, (chunks(q), chunks(k), chunks(v), chunks(log_f)))
    return jnp.moveaxis(o, 0, 2).reshape(b, h, t, dv).astype(v.dtype)


def bidirectional_scan(q, k_fw, k_bw, v, lf_fw, lf_bw, n_chunks):
    o_fw = chunk_scan(q, k_fw, v, lf_fw, n_chunks)
    o_bw = chunk_scan(segment_reverse(q), segment_reverse(k_bw), segment_reverse(v),
                      segment_reverse(lf_bw), n_chunks)
    return o_fw + segment_reverse(o_bw)


def hybrid_mixer(h, w_in, lb, hg_onorm, gla_w_gk, gla_b_gk, gla_onorm, w_br_hg, w_br_gla, w_out, n_chunks):
    (hq, hi, hf_fw, hf_bw, hg_gate, gq, gk, gv, g_gate,
     lr_fw, lr_bw, gate_hg, gate_gla) = jnp.split(h @ w_in, IN_OFFSETS, axis=-1)

    def hg_forget(raw, lb_dir):
        f = lb_dir + (1 - lb_dir) * jax.nn.sigmoid(raw.astype(jnp.float32))
        return to_heads(1 - f, HG_HEADS), to_heads(jnp.log(f), HG_HEADS)

    q = to_heads(jax.nn.silu(hq), HG_HEADS)
    i = to_heads(hi, HG_HEADS)
    k_fw, lf_fw = hg_forget(hf_fw, lb[0])
    k_bw, lf_bw = hg_forget(hf_bw, lb[1])
    o = bidirectional_scan(q, k_fw, k_bw, i, lf_fw, lf_bw, n_chunks)
    o_hg = merge_heads(rms_norm(o, hg_onorm)) * jax.nn.silu(hg_gate)

    def gla_gate_log(lr, w, bias):
        return to_heads(jax.nn.log_sigmoid((lr @ w + bias).astype(jnp.float32)) / GLA_GATE_NORM, GLA_HEADS)

    q = to_heads(gq, GLA_HEADS) * GLA_DK ** -0.5
    k = to_heads(gk, GLA_HEADS)
    v = to_heads(gv, GLA_HEADS)
    lf_fw = gla_gate_log(lr_fw, gla_w_gk[0], gla_b_gk[0])
    lf_bw = gla_gate_log(lr_bw, gla_w_gk[1], gla_b_gk[1])
    o = bidirectional_scan(q, k, k, v, lf_fw, lf_bw, n_chunks)
    o_gla = merge_heads(rms_norm(o, gla_onorm)) * jax.nn.silu(g_gate)

    merged = (jax.nn.sigmoid(gate_hg) * (o_hg @ w_br_hg)
              + jax.nn.sigmoid(gate_gla) * (o_gla @ w_br_gla))
    return merged @ w_out


def swiglu(h, w_gate, w_up, w_down):
    return (jax.nn.silu(h @ w_gate) * (h @ w_up)) @ w_down


def _fwd_setup_inputs(seed: int = 0) -> dict:
    key = jax.random.key(seed)
    ks = jax.random.split(key, 24)

    def nrm(k, shape, scale):
        return jax.random.normal(k, shape, jnp.float32) * scale

    def gain(k, shape):
        return 1.0 + nrm(k, shape, 0.05)

    return {
        "x": nrm(ks[0], (BATCH, SEQ, D_MODEL), 1.0),
        "c": nrm(ks[1], (BATCH, D_MODEL), 1.0),
        "ctx": nrm(ks[2], (BATCH, CTX_LEN, D_MODEL), 1.0),
        "c_ctx": nrm(ks[3], (D_MODEL,), 1.0),
        "w_mod": nrm(ks[4], (DEPTH, D_MODEL, N_MOD * D_MODEL), 0.5 * D_MODEL ** -0.5),
        "b_mod": nrm(ks[5], (DEPTH, N_MOD * D_MODEL), 0.01),
        "norm_pre1": gain(ks[6], (DEPTH, D_MODEL)),
        "norm_post1": gain(ks[7], (DEPTH, D_MODEL)),
        "norm_pre2": gain(ks[8], (DEPTH, D_MODEL)),
        "norm_post2": gain(ks[9], (DEPTH, D_MODEL)),
        "w_in": nrm(ks[10], (DEPTH, D_MODEL, IN_WIDTH), D_MODEL ** -0.5),
        "hg_lb": nrm(ks[11], (DEPTH + 1, 2, HG_W), 1.0),
        "hg_onorm": gain(ks[12], (DEPTH, HG_DV)),
        "gla_w_gk": nrm(ks[13], (DEPTH, 2, GLA_RANK, GLA_KW), GLA_RANK ** -0.5),
        "gla_b_gk": nrm(ks[14], (DEPTH, 2, GLA_KW), 0.1),
        "gla_onorm": gain(ks[15], (DEPTH, GLA_DV)),
        "w_br_hg": nrm(ks[16], (DEPTH, HG_VW, D_MODEL), HG_VW ** -0.5),
        "w_br_gla": nrm(ks[17], (DEPTH, GLA_VW, D_MODEL), GLA_VW ** -0.5),
        "w_out": nrm(ks[18], (DEPTH, D_MODEL, D_MODEL), D_MODEL ** -0.5),
        "w_ff_gate": nrm(ks[19], (DEPTH, D_MODEL, D_FF), D_MODEL ** -0.5),
        "w_ff_up": nrm(ks[20], (DEPTH, D_MODEL, D_FF), D_MODEL ** -0.5),
        "w_ff_down": nrm(ks[21], (DEPTH, D_FF, D_MODEL), D_FF ** -0.5),
    }


def _fwd_reference(x, c, ctx, c_ctx, w_mod, b_mod, norm_pre1, norm_post1, norm_pre2, norm_post2, w_in, hg_lb,
              hg_onorm, gla_w_gk, gla_b_gk, gla_onorm, w_br_hg, w_br_gla, w_out, w_ff_gate, w_ff_up, w_ff_down):
    rows = x.shape[1] // GRID_W
    n_chunks = CTX_LEN // GRID_W + rows
    z = jnp.concatenate([ctx, x], axis=1)
    lb_all = jnp.cumsum(jax.nn.softmax(hg_lb.astype(jnp.float32), axis=0), axis=0)
    for l in range(DEPTH):
        m_c = jnp.split(jax.nn.silu(c_ctx) @ w_mod[l] + b_mod[l], N_MOD, axis=-1)
        m_x = jnp.split(jax.nn.silu(c) @ w_mod[l] + b_mod[l], N_MOD, axis=-1)
        h = modulate(rms_norm(z, norm_pre1[l]), m_c[0], m_c[1], m_x[0], m_x[1])
        y = hybrid_mixer(h, w_in[l], lb_all[l], hg_onorm[l], gla_w_gk[l], gla_b_gk[l], gla_onorm[l],
                         w_br_hg[l], w_br_gla[l], w_out[l], n_chunks)
        z = z + apply_gate(rms_norm(y, norm_post1[l]), m_c[2], m_x[2])
        h = modulate(rms_norm(z, norm_pre2[l]), m_c[3], m_c[4], m_x[3], m_x[4])
        y = swiglu(h, w_ff_gate[l], w_ff_up[l], w_ff_down[l])
        z = z + apply_gate(rms_norm(y, norm_post2[l]), m_c[5], m_x[5])
    return z[:, CTX_LEN:]


import jax as _jax
import jax.numpy as _jnp

TWIN_FORMAT = 'train_step'
FWD_PARAMS = ['x', 'c', 'ctx', 'c_ctx', 'w_mod', 'b_mod', 'norm_pre1', 'norm_post1', 'norm_pre2', 'norm_post2', 'w_in', 'hg_lb', 'hg_onorm', 'gla_w_gk', 'gla_b_gk', 'gla_onorm', 'w_br_hg', 'w_br_gla', 'w_out', 'w_ff_gate', 'w_ff_up', 'w_ff_down']
TWIN_WEIGHTS = ['c_ctx', 'w_mod', 'b_mod', 'norm_pre1', 'norm_post1', 'norm_pre2', 'norm_post2', 'w_in', 'hg_lb', 'hg_onorm', 'gla_w_gk', 'gla_b_gk', 'gla_onorm', 'w_br_hg', 'w_br_gla', 'w_out', 'w_ff_gate', 'w_ff_up', 'w_ff_down']
TWIN_DIFF_INPUT = 'x'
TWIN_INPUTS = ['x', 'c', 'ctx', 'c_ctx', 'w_mod', 'b_mod', 'norm_pre1', 'norm_post1', 'norm_pre2', 'norm_post2', 'w_in', 'hg_lb', 'hg_onorm', 'gla_w_gk', 'gla_b_gk', 'gla_onorm', 'w_br_hg', 'w_br_gla', 'w_out', 'w_ff_gate', 'w_ff_up', 'w_ff_down', 'loss_target', 'm_c_ctx', 'm_w_mod', 'm_b_mod', 'm_norm_pre1', 'm_norm_post1', 'm_norm_pre2', 'm_norm_post2', 'm_w_in', 'm_hg_lb', 'm_hg_onorm', 'm_gla_w_gk', 'm_gla_b_gk', 'm_gla_onorm', 'm_w_br_hg', 'm_w_br_gla', 'm_w_out', 'm_w_ff_gate', 'm_w_ff_up', 'm_w_ff_down', 'v_c_ctx', 'v_w_mod', 'v_b_mod', 'v_norm_pre1', 'v_norm_post1', 'v_norm_pre2', 'v_norm_post2', 'v_w_in', 'v_hg_lb', 'v_hg_onorm', 'v_gla_w_gk', 'v_gla_b_gk', 'v_gla_onorm', 'v_w_br_hg', 'v_w_br_gla', 'v_w_out', 'v_w_ff_gate', 'v_w_ff_up', 'v_w_ff_down']
TWIN_OUTPUTS = ['loss', 'grad_x', 'grad_c_ctx', 'grad_w_mod', 'grad_b_mod', 'grad_norm_pre1', 'grad_norm_post1', 'grad_norm_pre2', 'grad_norm_post2', 'grad_w_in', 'grad_hg_lb', 'grad_hg_onorm', 'grad_gla_w_gk', 'grad_gla_b_gk', 'grad_gla_onorm', 'grad_w_br_hg', 'grad_w_br_gla', 'grad_w_out', 'grad_w_ff_gate', 'grad_w_ff_up', 'grad_w_ff_down', 'delta_c_ctx', 'delta_w_mod', 'delta_b_mod', 'delta_norm_pre1', 'delta_norm_post1', 'delta_norm_pre2', 'delta_norm_post2', 'delta_w_in', 'delta_hg_lb', 'delta_hg_onorm', 'delta_gla_w_gk', 'delta_gla_b_gk', 'delta_gla_onorm', 'delta_w_br_hg', 'delta_w_br_gla', 'delta_w_out', 'delta_w_ff_gate', 'delta_w_ff_up', 'delta_w_ff_down', 'new_m_c_ctx', 'new_m_w_mod', 'new_m_b_mod', 'new_m_norm_pre1', 'new_m_norm_post1', 'new_m_norm_pre2', 'new_m_norm_post2', 'new_m_w_in', 'new_m_hg_lb', 'new_m_hg_onorm', 'new_m_gla_w_gk', 'new_m_gla_b_gk', 'new_m_gla_onorm', 'new_m_w_br_hg', 'new_m_w_br_gla', 'new_m_w_out', 'new_m_w_ff_gate', 'new_m_w_ff_up', 'new_m_w_ff_down', 'new_v_c_ctx', 'new_v_w_mod', 'new_v_b_mod', 'new_v_norm_pre1', 'new_v_norm_post1', 'new_v_norm_pre2', 'new_v_norm_post2', 'new_v_w_in', 'new_v_hg_lb', 'new_v_hg_onorm', 'new_v_gla_w_gk', 'new_v_gla_b_gk', 'new_v_gla_onorm', 'new_v_w_br_hg', 'new_v_w_br_gla', 'new_v_w_out', 'new_v_w_ff_gate', 'new_v_w_ff_up', 'new_v_w_ff_down']
TWIN_LEAF_KINDS = {'loss': 'loss', 'grad_x': 'grad_x', 'grad_c_ctx': 'grad_w', 'grad_w_mod': 'grad_w', 'grad_b_mod': 'grad_w', 'grad_norm_pre1': 'grad_w', 'grad_norm_post1': 'grad_w', 'grad_norm_pre2': 'grad_w', 'grad_norm_post2': 'grad_w', 'grad_w_in': 'grad_w', 'grad_hg_lb': 'grad_w', 'grad_hg_onorm': 'grad_w', 'grad_gla_w_gk': 'grad_w', 'grad_gla_b_gk': 'grad_w', 'grad_gla_onorm': 'grad_w', 'grad_w_br_hg': 'grad_w', 'grad_w_br_gla': 'grad_w', 'grad_w_out': 'grad_w', 'grad_w_ff_gate': 'grad_w', 'grad_w_ff_up': 'grad_w', 'grad_w_ff_down': 'grad_w', 'delta_c_ctx': 'delta_w', 'delta_w_mod': 'delta_w', 'delta_b_mod': 'delta_w', 'delta_norm_pre1': 'delta_w', 'delta_norm_post1': 'delta_w', 'delta_norm_pre2': 'delta_w', 'delta_norm_post2': 'delta_w', 'delta_w_in': 'delta_w', 'delta_hg_lb': 'delta_w', 'delta_hg_onorm': 'delta_w', 'delta_gla_w_gk': 'delta_w', 'delta_gla_b_gk': 'delta_w', 'delta_gla_onorm': 'delta_w', 'delta_w_br_hg': 'delta_w', 'delta_w_br_gla': 'delta_w', 'delta_w_out': 'delta_w', 'delta_w_ff_gate': 'delta_w', 'delta_w_ff_up': 'delta_w', 'delta_w_ff_down': 'delta_w', 'new_m_c_ctx': 'new_m', 'new_m_w_mod': 'new_m', 'new_m_b_mod': 'new_m', 'new_m_norm_pre1': 'new_m', 'new_m_norm_post1': 'new_m', 'new_m_norm_pre2': 'new_m', 'new_m_norm_post2': 'new_m', 'new_m_w_in': 'new_m', 'new_m_hg_lb': 'new_m', 'new_m_hg_onorm': 'new_m', 'new_m_gla_w_gk': 'new_m', 'new_m_gla_b_gk': 'new_m', 'new_m_gla_onorm': 'new_m', 'new_m_w_br_hg': 'new_m', 'new_m_w_br_gla': 'new_m', 'new_m_w_out': 'new_m', 'new_m_w_ff_gate': 'new_m', 'new_m_w_ff_up': 'new_m', 'new_m_w_ff_down': 'new_m', 'new_v_c_ctx': 'new_v', 'new_v_w_mod': 'new_v', 'new_v_b_mod': 'new_v', 'new_v_norm_pre1': 'new_v', 'new_v_norm_post1': 'new_v', 'new_v_norm_pre2': 'new_v', 'new_v_norm_post2': 'new_v', 'new_v_w_in': 'new_v', 'new_v_hg_lb': 'new_v', 'new_v_hg_onorm': 'new_v', 'new_v_gla_w_gk': 'new_v', 'new_v_gla_b_gk': 'new_v', 'new_v_gla_onorm': 'new_v', 'new_v_w_br_hg': 'new_v', 'new_v_w_br_gla': 'new_v', 'new_v_w_out': 'new_v', 'new_v_w_ff_gate': 'new_v', 'new_v_w_ff_up': 'new_v', 'new_v_w_ff_down': 'new_v'}


def _forward(args):
    return _fwd_reference(*[args[k] for k in FWD_PARAMS])


def _output_shape():
    out = _jax.eval_shape(lambda: _forward(_fwd_setup_inputs(0)))
    return out.shape, out.dtype

N_MICROBATCH = 1
ADAM_LR = 0.001
ADAM_B1 = 0.9
ADAM_B2 = 0.999
ADAM_EPS = 1e-08
ADAM_WD = 0.01
ADAM_STEP = 10
PER_EXAMPLE_BATCH_AXIS = {'x': 0, 'c': 0, 'ctx': 0, 'loss_target': 0}
SHARED_INPUTS = []
_WEIGHT_DTYPES = {'c_ctx': _jnp.float32, 'w_mod': _jnp.float32, 'b_mod': _jnp.float32, 'norm_pre1': _jnp.float32, 'norm_post1': _jnp.float32, 'norm_pre2': _jnp.float32, 'norm_post2': _jnp.float32, 'w_in': _jnp.float32, 'hg_lb': _jnp.float32, 'hg_onorm': _jnp.float32, 'gla_w_gk': _jnp.float32, 'gla_b_gk': _jnp.float32, 'gla_onorm': _jnp.float32, 'w_br_hg': _jnp.float32, 'w_br_gla': _jnp.float32, 'w_out': _jnp.float32, 'w_ff_gate': _jnp.float32, 'w_ff_up': _jnp.float32, 'w_ff_down': _jnp.float32}
MOMENT_SCALE = {'c_ctx': 7.567115e-03, 'w_mod': 8.884431e-01, 'b_mod': 1.659140e+00, 'norm_pre1': 1.102945e-01, 'norm_post1': 1.883750e+00, 'norm_pre2': 8.844008e-02, 'norm_post2': 1.936914e+00, 'w_in': 5.096897e-02, 'hg_lb': 2.696099e-03, 'hg_onorm': 1.813014e-01, 'gla_w_gk': 8.202056e-03, 'gla_b_gk': 2.284071e-02, 'gla_onorm': 1.495879e-01, 'w_br_hg': 6.318068e-02, 'w_br_gla': 5.028195e-02, 'w_out': 8.440711e-02, 'w_ff_gate': 3.972795e-02, 'w_ff_up': 4.191955e-02, 'w_ff_down': 7.182272e-02}


def _to_microbatches(a, axis):
    t = _jnp.moveaxis(a, axis, 0)
    t = t.reshape((N_MICROBATCH, t.shape[0] // N_MICROBATCH) + t.shape[1:])
    return _jnp.moveaxis(t, 1, axis + 1)


def setup_inputs(seed: int = 0) -> dict:
    inp = _fwd_setup_inputs(seed)
    key = _jax.random.fold_in(_jax.random.key(seed), 7919)
    shape, _ = _output_shape()
    out = dict(inp)
    out["loss_target"] = _jax.random.normal(_jax.random.fold_in(key, 0), shape, _jnp.float32)
    for i, name in enumerate(TWIN_WEIGHTS):
        w = inp[name].astype(_jnp.float32)
        if MOMENT_SCALE is None:
            s = _jnp.sqrt(_jnp.mean(_jnp.square(w)) + 1e-30)
        else:
            s = MOMENT_SCALE[name]
        km, kv = _jax.random.split(_jax.random.fold_in(key, i + 1))
        out[name] = w
        out["m_" + name] = s * _jax.random.normal(km, w.shape, _jnp.float32)
        out["v_" + name] = (s * s) * _jax.random.uniform(kv, w.shape, _jnp.float32, 0.5, 1.5)
    if N_MICROBATCH > 1:
        for name, axis in PER_EXAMPLE_BATCH_AXIS.items():
            out[name] = _to_microbatches(out[name], axis)
    return {'x': out['x'], 'c': out['c'], 'ctx': out['ctx'], 'c_ctx': out['c_ctx'], 'w_mod': out['w_mod'], 'b_mod': out['b_mod'], 'norm_pre1': out['norm_pre1'], 'norm_post1': out['norm_post1'], 'norm_pre2': out['norm_pre2'], 'norm_post2': out['norm_post2'], 'w_in': out['w_in'], 'hg_lb': out['hg_lb'], 'hg_onorm': out['hg_onorm'], 'gla_w_gk': out['gla_w_gk'], 'gla_b_gk': out['gla_b_gk'], 'gla_onorm': out['gla_onorm'], 'w_br_hg': out['w_br_hg'], 'w_br_gla': out['w_br_gla'], 'w_out': out['w_out'], 'w_ff_gate': out['w_ff_gate'], 'w_ff_up': out['w_ff_up'], 'w_ff_down': out['w_ff_down'], 'loss_target': out['loss_target'], 'm_c_ctx': out['m_c_ctx'], 'm_w_mod': out['m_w_mod'], 'm_b_mod': out['m_b_mod'], 'm_norm_pre1': out['m_norm_pre1'], 'm_norm_post1': out['m_norm_post1'], 'm_norm_pre2': out['m_norm_pre2'], 'm_norm_post2': out['m_norm_post2'], 'm_w_in': out['m_w_in'], 'm_hg_lb': out['m_hg_lb'], 'm_hg_onorm': out['m_hg_onorm'], 'm_gla_w_gk': out['m_gla_w_gk'], 'm_gla_b_gk': out['m_gla_b_gk'], 'm_gla_onorm': out['m_gla_onorm'], 'm_w_br_hg': out['m_w_br_hg'], 'm_w_br_gla': out['m_w_br_gla'], 'm_w_out': out['m_w_out'], 'm_w_ff_gate': out['m_w_ff_gate'], 'm_w_ff_up': out['m_w_ff_up'], 'm_w_ff_down': out['m_w_ff_down'], 'v_c_ctx': out['v_c_ctx'], 'v_w_mod': out['v_w_mod'], 'v_b_mod': out['v_b_mod'], 'v_norm_pre1': out['v_norm_pre1'], 'v_norm_post1': out['v_norm_post1'], 'v_norm_pre2': out['v_norm_pre2'], 'v_norm_post2': out['v_norm_post2'], 'v_w_in': out['v_w_in'], 'v_hg_lb': out['v_hg_lb'], 'v_hg_onorm': out['v_hg_onorm'], 'v_gla_w_gk': out['v_gla_w_gk'], 'v_gla_b_gk': out['v_gla_b_gk'], 'v_gla_onorm': out['v_gla_onorm'], 'v_w_br_hg': out['v_w_br_hg'], 'v_w_br_gla': out['v_w_br_gla'], 'v_w_out': out['v_w_out'], 'v_w_ff_gate': out['v_w_ff_gate'], 'v_w_ff_up': out['v_w_ff_up'], 'v_w_ff_down': out['v_w_ff_down']}


def _loss(weights, diff, rest, loss_target):
    with _jax.named_scope("forward"):
        args = {**rest, TWIN_DIFF_INPUT: diff, **{k: w.astype(_WEIGHT_DTYPES[k]) for k, w in weights.items()}}
        y = _forward(args)
    with _jax.named_scope("loss_head"):
        err = _jnp.square(y.astype(_jnp.float32) - loss_target)
        return 0.5 * _jnp.sum(_jnp.mean(err, axis=-1)) if err.ndim else 0.5 * err


def _adamw(w, g, m, v):
    m = ADAM_B1 * m + (1.0 - ADAM_B1) * g
    v = ADAM_B2 * v + (1.0 - ADAM_B2) * _jnp.square(g)
    m_hat = m / (1.0 - ADAM_B1 ** ADAM_STEP)
    v_hat = v / (1.0 - ADAM_B2 ** ADAM_STEP)
    delta = -ADAM_LR * (m_hat / (_jnp.sqrt(v_hat) + ADAM_EPS) + ADAM_WD * w)
    return delta, m, v


def reference(x, c, ctx, c_ctx, w_mod, b_mod, norm_pre1, norm_post1, norm_pre2, norm_post2, w_in, hg_lb, hg_onorm, gla_w_gk, gla_b_gk, gla_onorm, w_br_hg, w_br_gla, w_out, w_ff_gate, w_ff_up, w_ff_down, loss_target, m_c_ctx, m_w_mod, m_b_mod, m_norm_pre1, m_norm_post1, m_norm_pre2, m_norm_post2, m_w_in, m_hg_lb, m_hg_onorm, m_gla_w_gk, m_gla_b_gk, m_gla_onorm, m_w_br_hg, m_w_br_gla, m_w_out, m_w_ff_gate, m_w_ff_up, m_w_ff_down, v_c_ctx, v_w_mod, v_b_mod, v_norm_pre1, v_norm_post1, v_norm_pre2, v_norm_post2, v_w_in, v_hg_lb, v_hg_onorm, v_gla_w_gk, v_gla_b_gk, v_gla_onorm, v_w_br_hg, v_w_br_gla, v_w_out, v_w_ff_gate, v_w_ff_up, v_w_ff_down):
    given = dict(x=x, c=c, ctx=ctx, c_ctx=c_ctx, w_mod=w_mod, b_mod=b_mod, norm_pre1=norm_pre1, norm_post1=norm_post1, norm_pre2=norm_pre2, norm_post2=norm_post2, w_in=w_in, hg_lb=hg_lb, hg_onorm=hg_onorm, gla_w_gk=gla_w_gk, gla_b_gk=gla_b_gk, gla_onorm=gla_onorm, w_br_hg=w_br_hg, w_br_gla=w_br_gla, w_out=w_out, w_ff_gate=w_ff_gate, w_ff_up=w_ff_up, w_ff_down=w_ff_down, loss_target=loss_target, m_c_ctx=m_c_ctx, m_w_mod=m_w_mod, m_b_mod=m_b_mod, m_norm_pre1=m_norm_pre1, m_norm_post1=m_norm_post1, m_norm_pre2=m_norm_pre2, m_norm_post2=m_norm_post2, m_w_in=m_w_in, m_hg_lb=m_hg_lb, m_hg_onorm=m_hg_onorm, m_gla_w_gk=m_gla_w_gk, m_gla_b_gk=m_gla_b_gk, m_gla_onorm=m_gla_onorm, m_w_br_hg=m_w_br_hg, m_w_br_gla=m_w_br_gla, m_w_out=m_w_out, m_w_ff_gate=m_w_ff_gate, m_w_ff_up=m_w_ff_up, m_w_ff_down=m_w_ff_down, v_c_ctx=v_c_ctx, v_w_mod=v_w_mod, v_b_mod=v_b_mod, v_norm_pre1=v_norm_pre1, v_norm_post1=v_norm_post1, v_norm_pre2=v_norm_pre2, v_norm_post2=v_norm_post2, v_w_in=v_w_in, v_hg_lb=v_hg_lb, v_hg_onorm=v_hg_onorm, v_gla_w_gk=v_gla_w_gk, v_gla_b_gk=v_gla_b_gk, v_gla_onorm=v_gla_onorm, v_w_br_hg=v_w_br_hg, v_w_br_gla=v_w_br_gla, v_w_out=v_w_out, v_w_ff_gate=v_w_ff_gate, v_w_ff_up=v_w_ff_up, v_w_ff_down=v_w_ff_down)
    weights = {n: given[n] for n in TWIN_WEIGHTS}
    shared = {n: given[n] for n in SHARED_INPUTS}
    per_example = {n: given[n] for n in ['x', 'c', 'ctx']}
    grad_fn = _jax.value_and_grad(_loss, argnums=(0, 1))

    def one_microbatch(ex, loss_target):
        ex = dict(ex)
        diff = ex.pop(TWIN_DIFF_INPUT)
        return grad_fn(weights, diff, {**shared, **ex}, loss_target)

    if N_MICROBATCH == 1:
        loss, (grad_w, grad_x) = one_microbatch(per_example, given["loss_target"])
    else:
        def body(carry, xs):
            loss_sum, grad_sum = carry
            l_k, (gw_k, gx_k) = one_microbatch(xs[0], xs[1])
            with _jax.named_scope("update"):
                return (loss_sum + l_k, _jax.tree.map(_jnp.add, grad_sum, gw_k)), gx_k

        init = (_jnp.zeros((), _jnp.float32), _jax.tree.map(_jnp.zeros_like, weights))
        (loss, grad_w), grad_x = _jax.lax.scan(body, init, (per_example, given["loss_target"]))
    with _jax.named_scope("update"):
        delta_w, new_m, new_v = {}, {}, {}
        for n in TWIN_WEIGHTS:
            delta_w[n], new_m[n], new_v[n] = _adamw(weights[n], grad_w[n], given["m_" + n], given["v_" + n])
    return (loss, grad_x, *[grad_w[n] for n in TWIN_WEIGHTS], *[delta_w[n] for n in TWIN_WEIGHTS],
            *[new_m[n] for n in TWIN_WEIGHTS], *[new_v[n] for n in TWIN_WEIGHTS])
```

```python
import functools

import jax
import jax.numpy as jnp
from jax import lax
from jax.experimental import pallas as pl
from jax.experimental.pallas import tpu as pltpu

F32 = jnp.float32
BF16 = jnp.bfloat16
HIGHEST = lax.Precision.HIGHEST
MESH = pl.DeviceIdType.MESH

EPS = 1e-6
CHUNK = 64
SUB = 16
NSUB = CHUNK // SUB
NH = 4
HD = 128
HW = NH * HD
RANK = 16
GATE_NORM = 16.0
N_MOD = 6
TM = 256
TM_FFN = 128
N_DEV = 8
N_CHIP = 4
VMEM_LIMIT = 56 * 1024 * 1024

ADAM_LR = 0.001
ADAM_B1 = 0.9
ADAM_B2 = 0.999
ADAM_EPS = 1e-08
ADAM_WD = 0.01
ADAM_STEP = 10

VMEM_SPEC = pl.BlockSpec(memory_space=pltpu.VMEM)
ANY_SPEC = pl.BlockSpec(memory_space=pl.ANY)


def _cparams(**kw):
    return pltpu.CompilerParams(vmem_limit_bytes=VMEM_LIMIT, **kw)


def _dot(a, b):
    return jnp.dot(a.astype(BF16), b.astype(BF16), preferred_element_type=F32)


def _dot_nt(a, b):
    return lax.dot_general(a.astype(BF16), b.astype(BF16), (((1,), (1,)), ((), ())), preferred_element_type=F32)


def _dot_tn(a, b):
    return lax.dot_general(a.astype(BF16), b.astype(BF16), (((0,), (0,)), ((), ())), preferred_element_type=F32)


def _sigmoid(x):
    return 1.0 / (1.0 + jnp.exp(-x))


def _silu(x):
    return x * _sigmoid(x)


def _dsilu(x):
    s = _sigmoid(x)
    return s * (1.0 + x * (1.0 - s))


def _log_sigmoid(x):
    return jnp.minimum(x, 0.0) - jnp.log(1.0 + jnp.exp(-jnp.abs(x)))


def _colsum(a):
    return jnp.sum(a, axis=0, keepdims=True)


def _rms(a):
    r = lax.rsqrt(jnp.mean(a * a, axis=-1, keepdims=True) + EPS)
    return a * r, r


def _rms_bwd(dn, n, r):
    return r * (dn - n * jnp.mean(dn * n, axis=-1, keepdims=True))


def _place():
    x, y, c = lax.axis_index("x"), lax.axis_index("y"), lax.axis_index("c")
    chips = [(1 - x, y), (x, 1 - y), (1 - x, 1 - y)]
    return x, y, c, chips


def _allgather8(v, name):
    rows, cols = v.shape

    def body(x_ref, out_ref, send_sems, recv_sems, local_sem):
        x, y, c, chips = _place()
        me, sibling = (x, y, c), (x, y, 1 - c)

        def blk(px, py, pc):
            return out_ref.at[4 * px + 2 * py + pc]

        def copy(k, block, to, src=None):
            return pltpu.make_async_remote_copy(
                src_ref=blk(*block) if src is None else src, dst_ref=blk(*block),
                send_sem=send_sems.at[k], recv_sem=recv_sems.at[k], device_id=to, device_id_type=MESH)

        mine = pltpu.make_async_copy(x_ref, blk(*me), local_sem)
        mine.start()
        first = [copy(0, me, sibling, src=x_ref)]
        first += [copy(1 + j, me, (*chip, c), src=x_ref) for j, chip in enumerate(chips)]
        for cp in first:
            cp.start()
        passed = [copy(4 + j, (*chip, c), sibling) for j, chip in enumerate(chips)]
        for j, chip in enumerate(chips):
            copy(1 + j, (*chip, c), me).wait_recv()
            passed[j].start()
        copy(0, sibling, me).wait_recv()
        for j, chip in enumerate(chips):
            copy(4 + j, (*chip, 1 - c), me).wait_recv()
        for cp in first + passed:
            cp.wait_send()
        mine.wait()

    return pl.pallas_call(
        body, name=name,
        out_shape=jax.ShapeDtypeStruct((N_DEV, rows, cols), v.dtype),
        in_specs=[VMEM_SPEC], out_specs=VMEM_SPEC,
        scratch_shapes=[pltpu.SemaphoreType.DMA((7,)), pltpu.SemaphoreType.DMA((7,)), pltpu.SemaphoreType.DMA],
    )(v)


def _gather_shards(shards, name):
    n = len(shards)

    def body(*refs):
        ins, outs = refs[:n], refs[n:2 * n]
        send_sems, recv_sems, local_sems = refs[2 * n:]
        x, y, c, chips = _place()
        me_chip = 2 * x + y
        sibling = (x, y, 1 - c)

        def half(k, chip_id, pc):
            h = shards[k].shape[0] // 2
            return outs[k].at[chip_id, pl.ds(pl.multiple_of(pc * h, 8), h), :]

        def copy(k, j, chip_id, pc, to, src=None):
            return pltpu.make_async_remote_copy(
                src_ref=half(k, chip_id, pc) if src is None else src, dst_ref=half(k, chip_id, pc),
                send_sem=send_sems.at[k, j], recv_sem=recv_sems.at[k, j], device_id=to, device_id_type=MESH)

        locals_ = []
        for k in range(n):
            cp = pltpu.make_async_copy(ins[k], outs[k].at[me_chip], local_sems.at[k])
            cp.start()
            locals_.append(cp)
        started = []
        for k in range(n):
            h = shards[k].shape[0] // 2
            src = ins[k].at[pl.ds(pl.multiple_of(c * h, 8), h), :]
            for j, chip in enumerate(chips):
                cp = copy(k, j, me_chip, c, (*chip, c), src=src)
                cp.start()
                started.append(cp)
        for k in range(n):
            for j, (px, py) in enumerate(chips):
                copy(k, j, 2 * px + py, c, sibling).wait_recv()
                cp = copy(k, 3 + j, 2 * px + py, c, sibling)
                cp.start()
                started.append(cp)
        for k in range(n):
            for j, (px, py) in enumerate(chips):
                copy(k, 3 + j, 2 * px + py, 1 - c, sibling).wait_recv()
        for cp in started:
            cp.wait_send()
        for cp in locals_:
            cp.wait()

    return pl.pallas_call(
        body, name=name,
        out_shape=[jax.ShapeDtypeStruct((N_CHIP,) + s.shape, s.dtype) for s in shards],
        in_specs=[ANY_SPEC] * n, out_specs=[ANY_SPEC] * n,
        scratch_shapes=[pltpu.SemaphoreType.DMA((n, 6)), pltpu.SemaphoreType.DMA((n, 6)),
                        pltpu.SemaphoreType.DMA((n,))],
    )(*shards)


def _send_other_half(arrs, name):
    n = len(arrs)

    def body(*refs):
        ins, outs = refs[:n], refs[n:2 * n]
        send_sems, recv_sems = refs[2 * n:]
        x, y, c, _ = _place()
        cps = []
        for k in range(n):
            h = arrs[k].shape[1] // 2
            cp = pltpu.make_async_remote_copy(
                src_ref=ins[k].at[:, pl.ds(pl.multiple_of((1 - c) * h, 8), h), :], dst_ref=outs[k],
                send_sem=send_sems.at[k], recv_sem=recv_sems.at[k], device_id=(x, y, 1 - c), device_id_type=MESH)
            cp.start()
            cps.append(cp)
        for cp in cps:
            cp.wait()

    return pl.pallas_call(
        body, name=name,
        out_shape=[jax.ShapeDtypeStruct((a.shape[0], a.shape[1] // 2, a.shape[2]), a.dtype) for a in arrs],
        in_specs=[ANY_SPEC] * n, out_specs=[ANY_SPEC] * n,
        scratch_shapes=[pltpu.SemaphoreType.DMA((n,)), pltpu.SemaphoreType.DMA((n,))],
    )(*arrs)


def _blocks_to_owner(arrs, name):
    n = len(arrs)

    def body(*refs):
        ins, outs = refs[:n], refs[n:2 * n]
        send_sems, recv_sems, local_sems = refs[2 * n:]
        x, y, c, chips = _place()
        me_chip = 2 * x + y
        locals_, started = [], []
        for k in range(n):
            cp = pltpu.make_async_copy(ins[k].at[me_chip], outs[k].at[me_chip], local_sems.at[k])
            cp.start()
            locals_.append(cp)

        def copy(k, j, src_block, dst_slot, to):
            return pltpu.make_async_remote_copy(
                src_ref=ins[k].at[src_block], dst_ref=outs[k].at[dst_slot],
                send_sem=send_sems.at[k, j], recv_sem=recv_sems.at[k, j], device_id=to, device_id_type=MESH)

        for k in range(n):
            for j, (px, py) in enumerate(chips):
                cp = copy(k, j, 2 * px + py, me_chip, (px, py, c))
                cp.start()
                started.append(cp)
        for k in range(n):
            for j, (px, py) in enumerate(chips):
                copy(k, j, me_chip, 2 * px + py, (px, py, c)).wait_recv()
        for cp in started:
            cp.wait_send()
        for cp in locals_:
            cp.wait()

    return pl.pallas_call(
        body, name=name,
        out_shape=[jax.ShapeDtypeStruct(a.shape, a.dtype) for a in arrs],
        in_specs=[ANY_SPEC] * n, out_specs=[ANY_SPEC] * n,
        scratch_shapes=[pltpu.SemaphoreType.DMA((n, 3)), pltpu.SemaphoreType.DMA((n, 3)),
                        pltpu.SemaphoreType.DMA((n,))],
    )(*arrs)


def _swap_with_sibling(arrs, name):
    n = len(arrs)

    def body(*refs):
        ins, outs = refs[:n], refs[n:2 * n]
        send_sems, recv_sems = refs[2 * n:]
        x, y, c, _ = _place()
        cps = []
        for k in range(n):
            cp = pltpu.make_async_remote_copy(
                src_ref=ins[k], dst_ref=outs[k], send_sem=send_sems.at[k], recv_sem=recv_sems.at[k],
                device_id=(x, y, 1 - c), device_id_type=MESH)
            cp.start()
            cps.append(cp)
        for cp in cps:
            cp.wait()

    return pl.pallas_call(
        body, name=name,
        out_shape=[jax.ShapeDtypeStruct(a.shape, a.dtype) for a in arrs],
        in_specs=[ANY_SPEC] * n, out_specs=[ANY_SPEC] * n,
        scratch_shapes=[pltpu.SemaphoreType.DMA((n,)), pltpu.SemaphoreType.DMA((n,))],
    )(*arrs)


def _row_tile(h):
    for t in range(128, 7, -8):
        if h % t == 0:
            return t
    raise ValueError(h)


def _cast_bf16(a, name):
    rows, cols = a.shape
    tr = _row_tile(rows)

    def body(a_ref, o_ref):
        o_ref[...] = a_ref[...].astype(BF16)

    return pl.pallas_call(
        body, name=name, grid=(rows // tr,),
        out_shape=jax.ShapeDtypeStruct(a.shape, BF16),
        in_specs=[pl.BlockSpec((tr, cols), lambda i: (i, 0))],
        out_specs=pl.BlockSpec((tr, cols), lambda i: (i, 0)),
        compiler_params=_cparams(dimension_semantics=("parallel",)),
    )(a)


def _pair_sum(c_arr, full, recv, name):
    nb, rows, cols = full.shape
    h = rows // 2
    tr = _row_tile(h)
    steps = h // tr

    def body(c_ref, f_ref, r_ref, o_ref):
        o_ref[...] = f_ref[...] + r_ref[...]

    return pl.pallas_call(
        body, name=name,
        grid_spec=pltpu.PrefetchScalarGridSpec(
            num_scalar_prefetch=1, grid=(nb, steps),
            in_specs=[pl.BlockSpec((1, tr, cols), lambda b, i, c_ref: (b, c_ref[0] * steps + i, 0)),
                      pl.BlockSpec((1, tr, cols), lambda b, i, c_ref: (b, i, 0))],
            out_specs=pl.BlockSpec((1, tr, cols), lambda b, i, c_ref: (b, i, 0))),
        out_shape=jax.ShapeDtypeStruct((nb, h, cols), F32),
        compiler_params=_cparams(dimension_semantics=("parallel", "parallel")),
    )(c_arr, full, recv)


def _sum_chips(got, name):
    nb, h, cols = got.shape
    tr = _row_tile(h)

    def body(g_ref, o_ref):
        o_ref[...] = ((g_ref[0] + g_ref[1]) + g_ref[2]) + g_ref[3]

    return pl.pallas_call(
        body, name=name, grid=(h // tr,),
        out_shape=jax.ShapeDtypeStruct((h, cols), F32),
        in_specs=[pl.BlockSpec((nb, tr, cols), lambda i: (0, i, 0))],
        out_specs=pl.BlockSpec((tr, cols), lambda i: (i, 0)),
        compiler_params=_cparams(dimension_semantics=("parallel",)),
    )(got)


def _adam_math(g, w, m, v):
    m1 = ADAM_B1 * m + (1.0 - ADAM_B1) * g
    v1 = ADAM_B2 * v + (1.0 - ADAM_B2) * (g * g)
    m_hat = m1 / (1.0 - ADAM_B1 ** ADAM_STEP)
    v_hat = v1 / (1.0 - ADAM_B2 ** ADAM_STEP)
    delta = -ADAM_LR * (m_hat / (jnp.sqrt(v_hat) + ADAM_EPS) + ADAM_WD * w)
    return delta, m1, v1


def _adamw_halves(c_arr, own, other, w, m, v, name):
    rows, cols = w.shape
    h = rows // 2
    tr = _row_tile(h)
    steps = h // tr

    def body(c_ref, own_ref, oth_ref, w_ref, m_ref, v_ref, g_out, d_out, m_out, v_out):
        g = jnp.where(pl.program_id(0) == c_ref[0], own_ref[...], oth_ref[...])
        d, m1, v1 = _adam_math(g, w_ref[...], m_ref[...], v_ref[...])
        g_out[...] = g
        d_out[...] = d
        m_out[...] = m1
        v_out[...] = v1

    half_spec = pl.BlockSpec((tr, cols), lambda p, i, c_ref: (i, 0))
    full_spec = pl.BlockSpec((tr, cols), lambda p, i, c_ref: (p * steps + i, 0))
    return pl.pallas_call(
        body, name=name,
        grid_spec=pltpu.PrefetchScalarGridSpec(
            num_scalar_prefetch=1, grid=(2, steps),
            in_specs=[half_spec, half_spec, full_spec, full_spec, full_spec],
            out_specs=[full_spec] * 4),
        out_shape=[jax.ShapeDtypeStruct(w.shape, F32)] * 4,
        compiler_params=_cparams(dimension_semantics=("parallel", "parallel")),
    )(c_arr, own, other, w, m, v)


def _adamw_whole(items, name):
    n = len(items)

    def body(*refs):
        ins, outs = refs[:4 * n], refs[4 * n:]
        for k in range(n):
            g, w, m, v = (r[...] for r in ins[4 * k:4 * k + 4])
            d, m1, v1 = _adam_math(g, w, m, v)
            outs[3 * k][...] = d
            outs[3 * k + 1][...] = m1
            outs[3 * k + 2][...] = v1

    flat = [a for it in items for a in it]
    shapes = [jax.ShapeDtypeStruct(it[1].shape, F32) for it in items for _ in range(3)]
    out = pl.pallas_call(
        body, name=name, out_shape=shapes,
        in_specs=[VMEM_SPEC] * (4 * n), out_specs=[VMEM_SPEC] * (3 * n),
        compiler_params=_cparams(),
    )(*flat)
    return [tuple(out[3 * k:3 * k + 3]) for k in range(n)]


def _adamw_tiled(g, w, m, v, name):
    rows, cols = w.shape
    tr = _row_tile(rows)

    def body(g_ref, w_ref, m_ref, v_ref, d_out, m_out, v_out):
        d, m1, v1 = _adam_math(g_ref[...], w_ref[...], m_ref[...], v_ref[...])
        d_out[...] = d
        m_out[...] = m1
        v_out[...] = v1

    spec = pl.BlockSpec((tr, cols), lambda i: (i, 0))
    return pl.pallas_call(
        body, name=name, grid=(rows // tr,),
        out_shape=[jax.ShapeDtypeStruct(w.shape, F32)] * 3,
        in_specs=[spec] * 4, out_specs=[spec] * 3,
        compiler_params=_cparams(dimension_semantics=("parallel",)),
    )(g, w, m, v)


def _mod_forward(cond, w_mod, b_mod_cols, name):
    def body(c_ref, w_ref, b_ref, o_ref):
        o_ref[...] = _dot(_silu(c_ref[...]), w_ref[...]) + b_ref[...]

    return pl.pallas_call(
        body, name=name, out_shape=jax.ShapeDtypeStruct((cond.shape[0], w_mod.shape[1]), F32),
        in_specs=[VMEM_SPEC] * 3, out_specs=VMEM_SPEC, compiler_params=_cparams(),
    )(cond, w_mod, b_mod_cols)


def _mod_backward(cond, w_mod, dmod_cols, name):
    def body(c_ref, w_ref, d_ref, gw_ref, gc_ref):
        s = _silu(c_ref[...])
        d = d_ref[...]
        gw_ref[...] = _dot_tn(s, d)
        gc_ref[...] = _dot_nt(d[8:16, :], w_ref[...])

    return pl.pallas_call(
        body, name=name,
        out_shape=[jax.ShapeDtypeStruct(w_mod.shape, F32), jax.ShapeDtypeStruct((8, w_mod.shape[0]), F32)],
        in_specs=[VMEM_SPEC] * 3, out_specs=[VMEM_SPEC] * 2, compiler_params=_cparams(),
    )(cond, w_mod, dmod_cols)


def _col_chunks(width, step=512):
    return [(s, min(step, width - s)) for s in range(0, width, step)]


def _in_projection(z, modc, modx, pre1, w_r, n_ctx_tiles, name):
    rows, d = z.shape
    width = w_r.shape[1]

    def body(z_ref, modc_ref, modx_ref, pre_ref, w_ref, h_ref, p_ref):
        is_ctx = pl.program_id(0) < n_ctx_tiles
        n, _ = _rms(z_ref[...])
        shift = jnp.where(is_ctx, modc_ref[0:1, :], modx_ref[0:1, :])
        scale = jnp.where(is_ctx, modc_ref[1:2, :], modx_ref[1:2, :])
        h = (n * pre_ref[...] * (1.0 + scale) + shift).astype(BF16)
        h_ref[...] = h
        for s, w in _col_chunks(width):
            p_ref[:, s:s + w] = jnp.dot(h, w_ref[:, s:s + w], preferred_element_type=F32)

    row = lambda i: (i, 0)
    fixed = lambda i: (0, 0)
    return pl.pallas_call(
        body, name=name, grid=(rows // TM,),
        out_shape=[jax.ShapeDtypeStruct((rows, d), BF16), jax.ShapeDtypeStruct((rows, width), F32)],
        in_specs=[pl.BlockSpec((TM, d), row), pl.BlockSpec((8, d), fixed), pl.BlockSpec((8, d), fixed),
                  pl.BlockSpec((1, d), fixed), VMEM_SPEC],
        out_specs=[pl.BlockSpec((TM, d), row), pl.BlockSpec((TM, width), row)],
        compiler_params=_cparams(dimension_semantics=("parallel",)),
    )(z, modc, modx, pre1, w_r)


C_HQ, C_HI, C_HF_FW, C_HF_BW, C_HGATE, C_GQ, C_GK, C_GV, C_GGATE = range(9)
OFF_GATE_HG = 9 * HW
OFF_LR = 13 * HW
P_WIDTH = OFF_LR + 128


def _head_norm_fwd(o, w):
    outs, ns, rs = [], [], []
    for h in range(NH):
        n, r = _rms(o[:, h * HD:(h + 1) * HD])
        ns.append(n)
        rs.append(r)
        outs.append(n * w)
    return jnp.concatenate(outs, axis=1), ns, rs


def _mixer_tail(z, o_hg, o_gla, p_hgate, p_ggate, p_gate_hg, p_gate_gla, hg_on, gla_on, wbh, wbg, wout):
    on_hg, n_hg, r_hg = _head_norm_fwd(o_hg, hg_on)
    on_gla, n_gla, r_gla = _head_norm_fwd(o_gla, gla_on)
    og_hg = (on_hg * _silu(p_hgate)).astype(BF16)
    og_gla = (on_gla * _silu(p_ggate)).astype(BF16)
    b_hg = jnp.dot(og_hg, wbh, preferred_element_type=F32)
    b_gla = jnp.dot(og_gla, wbg, preferred_element_type=F32)
    s_hg = _sigmoid(p_gate_hg)
    s_gla = _sigmoid(p_gate_gla)
    merged = (s_hg * b_hg + s_gla * b_gla).astype(BF16)
    y1 = jnp.dot(merged, wout, preferred_element_type=F32)
    return dict(on_hg=on_hg, n_hg=n_hg, r_hg=r_hg, on_gla=on_gla, n_gla=n_gla, r_gla=r_gla, og_hg=og_hg,
                og_gla=og_gla, b_hg=b_hg, b_gla=b_gla, s_hg=s_hg, s_gla=s_gla, merged=merged, y1=y1)


def _mixer_ffn(x_lat, p, o_list, modx, norms, onorms, w_br_hg, w_br_gla, w_out, w_gate, w_up, w_down, target,
               n_ctx_tiles, name):
    rows, d = x_lat.shape
    dff = w_gate.shape[1]
    inv_d = 1.0 / d

    def body(x_ref, ofw_hg, obw_hg, ofw_gla, obw_gla, p_hgate, p_ggate, p_ghg_a, p_ghg_b, p_ggla_a, p_ggla_b,
             modx_ref, norm_ref, on_ref, wbh_ref, wbg_ref, wout_ref, wg_ref, wu_ref, wd_ref, t_ref,
             loss_ref, dz2_ref, y1_ref, mrg_ref, oghg_ref, oggla_ref, h2_ref, a_ref, du_ref, dv_ref, dy2_ref,
             stat_ref):
        i = pl.program_id(0)
        post1, pre2, post2 = norm_ref[1:2, :], norm_ref[2:3, :], norm_ref[3:4, :]
        gate1, shift2, scale2, gate2 = modx_ref[2:3, :], modx_ref[3:4, :], modx_ref[4:5, :], modx_ref[5:6, :]
        p_gate_hg = jnp.concatenate([p_ghg_a[...], p_ghg_b[...]], axis=1)
        p_gate_gla = jnp.concatenate([p_ggla_a[...], p_ggla_b[...]], axis=1)
        t = _mixer_tail(x_ref[...], ofw_hg[...] + obw_hg[...], ofw_gla[...] + obw_gla[...], p_hgate[...],
                        p_ggate[...], p_gate_hg, p_gate_gla, on_ref[0:1, 0:HD], on_ref[1:2, 0:HD],
                        wbh_ref[...], wbg_ref[...], wout_ref[...])
        y1_ref[...] = t["y1"]
        mrg_ref[...] = t["merged"]
        oghg_ref[...] = t["og_hg"]
        oggla_ref[...] = t["og_gla"]
        n1, _ = _rms(t["y1"])
        z2 = x_ref[...] + n1 * post1 * gate1
        n2, r2 = _rms(z2)
        nw2 = n2 * pre2
        h2 = (nw2 * (1.0 + scale2) + shift2).astype(BF16)
        h2_ref[...] = h2
        u = jnp.dot(h2, wg_ref[...], preferred_element_type=F32)
        v = jnp.dot(h2, wu_ref[...], preferred_element_type=F32)
        su = _silu(u)
        a = (su * v).astype(BF16)
        a_ref[...] = a
        y2 = jnp.dot(a, wd_ref[...], preferred_element_type=F32)
        n3, r3 = _rms(y2)
        z3 = z2 + n3 * post2 * gate2
        err = z3 - t_ref[...]
        part = 0.5 * inv_d * jnp.sum(err * err)
        dz3 = err * inv_d
        dgate2 = _colsum(dz3 * n3 * post2)
        tt = dz3 * gate2
        dpost2 = _colsum(tt * n3)
        dy2 = _rms_bwd(tt * post2, n3, r3).astype(BF16)
        dy2_ref[...] = dy2
        da = _dot_nt(dy2, wd_ref[...])
        du = (da * v * _dsilu(u)).astype(BF16)
        dv = (da * su).astype(BF16)
        du_ref[...] = du
        dv_ref[...] = dv
        dh2 = _dot_nt(du, wg_ref[...]) + _dot_nt(dv, wu_ref[...])
        dshift2 = _colsum(dh2)
        dscale2 = _colsum(dh2 * nw2)
        dnw2 = dh2 * (1.0 + scale2)
        dpre2 = _colsum(dnw2 * n2)
        dz2_ref[...] = dz3 + _rms_bwd(dnw2 * pre2, n2, r2)

        @pl.when(i == 0)
        def _():
            stat_ref[...] = jnp.zeros_like(stat_ref)
            loss_ref[...] = jnp.zeros_like(loss_ref)

        for r, val in enumerate((dshift2, dscale2, dgate2, dpre2, dpost2)):
            stat_ref[r:r + 1, :] += val
        loss_ref[...] += part

    tm = TM_FFN
    ctx_tiles = n_ctx_tiles * (TM // tm)
    lat = lambda i: (i, 0)
    full = lambda i: (i + ctx_tiles, 0)
    fixed = lambda i: (0, 0)

    def pcol(blk):
        return pl.BlockSpec((tm, HW), lambda i: (i + ctx_tiles, blk))

    in_specs = ([pl.BlockSpec((tm, d), lat)] + [pl.BlockSpec((tm, HW), full)] * 4
                + [pcol(C_HGATE), pcol(C_GGATE), pcol(9), pcol(10), pcol(11), pcol(12)]
                + [pl.BlockSpec((8, d), fixed), pl.BlockSpec((8, d), fixed), pl.BlockSpec((8, d), fixed)]
                + [VMEM_SPEC] * 6 + [pl.BlockSpec((tm, d), lat)])
    bf = lambda w: jax.ShapeDtypeStruct((rows, w), BF16)
    out_shape = [jax.ShapeDtypeStruct((8, 128), F32), jax.ShapeDtypeStruct((rows, d), F32),
                 jax.ShapeDtypeStruct((rows, d), F32), bf(d), bf(HW), bf(HW), bf(d), bf(dff), bf(dff), bf(dff), bf(d),
                 jax.ShapeDtypeStruct((8, d), F32)]
    out_specs = [pl.BlockSpec((8, 128), fixed), pl.BlockSpec((tm, d), lat), pl.BlockSpec((tm, d), lat),
                 pl.BlockSpec((tm, d), lat), pl.BlockSpec((tm, HW), lat), pl.BlockSpec((tm, HW), lat),
                 pl.BlockSpec((tm, d), lat), pl.BlockSpec((tm, dff), lat), pl.BlockSpec((tm, dff), lat),
                 pl.BlockSpec((tm, dff), lat), pl.BlockSpec((tm, d), lat), pl.BlockSpec((8, d), fixed)]
    return pl.pallas_call(
        body, name=name, grid=(rows // tm,), out_shape=out_shape, in_specs=in_specs, out_specs=out_specs,
        compiler_params=_cparams(dimension_semantics=("arbitrary",)),
    )(x_lat, *o_list, p, p, p, p, p, p, modx, norms, onorms, w_br_hg, w_br_gla, w_out, w_gate, w_up, w_down, target)


def _mixer_tail_bwd(x_lat, p, o_list, dz2, y1, modx, norms, onorms, w_br_hg, w_br_gla, w_out, n_ctx_tiles, n_tiles,
                    name):
    rows, d = x_lat.shape
    total = n_tiles * TM

    def body(x_ref, ofw_hg, obw_hg, ofw_gla, obw_gla, p_hgate, p_ggate, p_ghg_a, p_ghg_b, p_ggla_a, p_ggla_b,
             dz2_ref, y1_ref, modx_ref, norm_ref, on_ref, wbh_ref, wbg_ref, wout_ref,
             dohg_ref, dogla_ref, dhgate_ref, dggate_ref, dghg_ref, dggla_ref, dy1_ref, dbhg_ref, dbgla_ref,
             stat_ref):
        i = pl.program_id(0)

        @pl.when(i == 0)
        def _():
            stat_ref[...] = jnp.zeros_like(stat_ref)

        @pl.when(i < n_ctx_tiles)
        def _():
            for ref in (dohg_ref, dogla_ref, dhgate_ref, dggate_ref, dghg_ref, dggla_ref):
                ref[...] = jnp.zeros_like(ref)

        @pl.when(i >= n_ctx_tiles)
        def _():
            post1, gate1 = norm_ref[1:2, :], modx_ref[2:3, :]
            hg_on, gla_on = on_ref[0:1, 0:HD], on_ref[1:2, 0:HD]
            p_gate_hg = jnp.concatenate([p_ghg_a[...], p_ghg_b[...]], axis=1)
            p_gate_gla = jnp.concatenate([p_ggla_a[...], p_ggla_b[...]], axis=1)
            ph, pg = p_hgate[...], p_ggate[...]
            t = _mixer_tail(x_ref[...], ofw_hg[...] + obw_hg[...], ofw_gla[...] + obw_gla[...], ph, pg,
                            p_gate_hg, p_gate_gla, hg_on, gla_on, wbh_ref[...], wbg_ref[...], wout_ref[...])
            dz2 = dz2_ref[...]
            n1, r1 = _rms(y1_ref[...])
            dgate1 = _colsum(dz2 * n1 * post1)
            tt = dz2 * gate1
            dpost1 = _colsum(tt * n1)
            dy1 = _rms_bwd(tt * post1, n1, r1).astype(BF16)
            dy1_ref[...] = dy1
            dmerged = _dot_nt(dy1, wout_ref[...])
            dghg_ref[...] = dmerged * t["b_hg"] * t["s_hg"] * (1.0 - t["s_hg"])
            dggla_ref[...] = dmerged * t["b_gla"] * t["s_gla"] * (1.0 - t["s_gla"])
            db_hg = (dmerged * t["s_hg"]).astype(BF16)
            db_gla = (dmerged * t["s_gla"]).astype(BF16)
            dbhg_ref[...] = db_hg
            dbgla_ref[...] = db_gla
            don_acc = []
            for (db, wb, pgate, on, ns, rs, gain, gate_ref, do_ref) in (
                    (db_hg, wbh_ref, ph, t["on_hg"], t["n_hg"], t["r_hg"], hg_on, dhgate_ref, dohg_ref),
                    (db_gla, wbg_ref, pg, t["on_gla"], t["n_gla"], t["r_gla"], gla_on, dggate_ref, dogla_ref)):
                dog = _dot_nt(db, wb[...])
                gate_ref[...] = dog * on * _dsilu(pgate)
                don = dog * _silu(pgate)
                acc = jnp.zeros((1, HD), F32)
                for h in range(NH):
                    sl = slice(h * HD, (h + 1) * HD)
                    acc = acc + _colsum(don[:, sl] * ns[h])
                    do_ref[:, sl] = _rms_bwd(don[:, sl] * gain, ns[h], rs[h])
                don_acc.append(acc)
            stat_ref[0:1, :] += dgate1
            stat_ref[1:2, :] += dpost1
            stat_ref[2:3, 0:HD] += don_acc[0]
            stat_ref[2:3, HD:2 * HD] += don_acc[1]

    lat = lambda i: (jnp.maximum(i - n_ctx_tiles, 0), 0)
    full = lambda i: (i, 0)
    fixed = lambda i: (0, 0)

    def pcol(blk):
        return pl.BlockSpec((TM, HW), lambda i: (i, blk))

    in_specs = ([pl.BlockSpec((TM, d), lat)] + [pl.BlockSpec((TM, HW), full)] * 4
                + [pcol(C_HGATE), pcol(C_GGATE), pcol(9), pcol(10), pcol(11), pcol(12)]
                + [pl.BlockSpec((TM, d), lat), pl.BlockSpec((TM, d), lat)]
                + [pl.BlockSpec((8, d), fixed)] * 3 + [VMEM_SPEC] * 3)
    f = lambda w: jax.ShapeDtypeStruct((total, w), F32)
    out_shape = [f(HW), f(HW), f(HW), f(HW), f(d), f(d), jax.ShapeDtypeStruct((rows, d), BF16),
                 jax.ShapeDtypeStruct((rows, d), BF16), jax.ShapeDtypeStruct((rows, d), BF16),
                 jax.ShapeDtypeStruct((8, d), F32)]
    out_specs = ([pl.BlockSpec((TM, HW), full)] * 4 + [pl.BlockSpec((TM, d), full)] * 2
                 + [pl.BlockSpec((TM, d), lat)] * 3 + [pl.BlockSpec((8, d), fixed)])
    return pl.pallas_call(
        body, name=name, grid=(n_tiles,), out_shape=out_shape, in_specs=in_specs, out_specs=out_specs,
        compiler_params=_cparams(dimension_semantics=("arbitrary",)),
    )(x_lat, *o_list, p, p, p, p, p, p, dz2, y1, modx, norms, onorms, w_br_hg, w_br_gla, w_out)


def _in_projection_bwd(z, dz2, modc, modx, pre1, w_r, pieces, n_ctx_tiles, name):
    rows, d = z.shape
    lat_rows = dz2.shape[0]
    width = w_r.shape[1]
    n_pieces = len(pieces)

    def body(*refs):
        z_ref, dz2_ref, modc_ref, modx_ref, pre_ref, w_ref = refs[:6]
        (dhq_f, dhq_b, dhi_f, dhi_b, dhf_f, dhf_b, dhgate, dgq_f, dgq_b, dgk_f, dgk_b, dgv_f, dgv_b, dggate,
         dghg, dggla, dlr_f, dlr_b) = refs[6:6 + n_pieces]
        dp_ref, gx_ref, stat_ref = refs[6 + n_pieces:]
        i = pl.program_id(0)
        is_ctx = i < n_ctx_tiles
        sections = [
            (0, dhq_f[...] + dhq_b[...]), (HW, dhi_f[...] + dhi_b[...]), (2 * HW, dhf_f[...]), (3 * HW, dhf_b[...]),
            (4 * HW, dhgate[...]), (5 * HW, dgq_f[...] + dgq_b[...]), (6 * HW, dgk_f[...] + dgk_b[...]),
            (7 * HW, dgv_f[...] + dgv_b[...]), (8 * HW, dggate[...]),
            (9 * HW, dghg[:, 0:HW]), (10 * HW, dghg[:, HW:2 * HW]),
            (11 * HW, dggla[:, 0:HW]), (12 * HW, dggla[:, HW:2 * HW]), (OFF_LR, dlr_f[...] + dlr_b[...])]
        dh = jnp.zeros((TM, d), F32)
        for off, val in sections:
            w = val.shape[1]
            vb = val.astype(BF16)
            dp_ref[:, off:off + w] = vb
            dh = dh + _dot_nt(vb, w_ref[:, off:off + w])
        n, r = _rms(z_ref[...])
        pre = pre_ref[...]
        scale = jnp.where(is_ctx, modc_ref[1:2, :], modx_ref[1:2, :])
        nw = n * pre
        dshift = _colsum(dh)
        dscale = _colsum(dh * nw)
        dnw = dh * (1.0 + scale)
        dpre = _colsum(dnw * n)
        gx_ref[...] = dz2_ref[...] + _rms_bwd(dnw * pre, n, r)
        zero = jnp.zeros((1, d), F32)

        @pl.when(i == 0)
        def _():
            stat_ref[...] = jnp.zeros_like(stat_ref)

        stat_ref[0:1, :] += jnp.where(is_ctx, zero, dshift)
        stat_ref[1:2, :] += jnp.where(is_ctx, zero, dscale)
        stat_ref[2:3, :] += jnp.where(is_ctx, dshift, zero)
        stat_ref[3:4, :] += jnp.where(is_ctx, dscale, zero)
        stat_ref[4:5, :] += dpre

    full = lambda i: (i, 0)
    lat = lambda i: (jnp.maximum(i - n_ctx_tiles, 0), 0)
    fixed = lambda i: (0, 0)
    piece_specs = [pl.BlockSpec((TM, a.shape[1]), full) for a in pieces]
    in_specs = [pl.BlockSpec((TM, d), full), pl.BlockSpec((TM, d), lat), pl.BlockSpec((8, d), fixed),
                pl.BlockSpec((8, d), fixed), pl.BlockSpec((1, d), fixed), VMEM_SPEC] + piece_specs
    return pl.pallas_call(
        body, name=name, grid=(rows // TM,),
        out_shape=[jax.ShapeDtypeStruct((rows, width), BF16), jax.ShapeDtypeStruct((lat_rows, d), F32),
                   jax.ShapeDtypeStruct((8, d), F32)],
        in_specs=in_specs,
        out_specs=[pl.BlockSpec((TM, width), full), pl.BlockSpec((TM, d), lat), pl.BlockSpec((8, d), fixed)],
        compiler_params=_cparams(dimension_semantics=("arbitrary",)),
    )(z, dz2, modc, modx, pre1, w_r, *pieces)


def _weight_grad(xs, dy, name, col_block=None, tk=256, tn=512):
    rows = dy.shape[0]
    k = xs.shape[1]
    if col_block is None:
        n, tn_, cb = dy.shape[1], tn, 0
    else:
        n, tn_, cb = col_block[0], col_block[0], col_block[1]
    tn_ = min(tn_, n)
    tk_ = min(tk, k)

    def body(x_ref, dy_ref, o_ref):
        o_ref[...] = _dot_tn(x_ref[...], dy_ref[...])

    return pl.pallas_call(
        body, name=name, grid=(k // tk_, n // tn_),
        out_shape=jax.ShapeDtypeStruct((k, n), F32),
        in_specs=[pl.BlockSpec((rows, tk_), lambda i, j: (0, i)),
                  pl.BlockSpec((rows, tn_), lambda i, j: (0, j + cb))],
        out_specs=pl.BlockSpec((tk_, tn_), lambda i, j: (i, j)),
        compiler_params=_cparams(dimension_semantics=("parallel", "parallel")),
    )(xs, dy)


def _chunk_terms(q, k, g, fw):
    c = CHUNK
    r = lax.broadcasted_iota(jnp.int32, (c, c), 0)
    s = lax.broadcasted_iota(jnp.int32, (c, c), 1)
    causal = (s <= r) if fw else (s >= r)
    causal_t = (s >= r) if fw else (s <= r)
    cum = jnp.dot(causal.astype(F32), g, precision=HIGHEST, preferred_element_type=F32)
    row = lax.broadcasted_iota(jnp.int32, (c, 1), 0)
    pos = row if fw else (c - 1 - row)
    starts = [None]
    for j in range(1, NSUB):
        rj = SUB * j - 1 if fw else c - SUB * j
        starts.append(cum[rj:rj + 1, :])
    in_blk = [(pos >= SUB * j) & (pos < SUB * (j + 1)) for j in range(NSUB)]
    e = [jnp.exp(cum)]
    for j in range(1, NSUB):
        e.append(jnp.exp(jnp.where(pos >= SUB * j, cum - starts[j], -1e30)))
    own = jnp.zeros_like(cum)
    for j in range(1, NSUB):
        own = own + jnp.where(in_blk[j], starts[j], 0.0)
    kscale = jnp.exp(own - cum)
    rend = c - 1 if fw else 0
    cend = cum[rend:rend + 1, :]
    tail = jnp.exp(cend - cum)
    qcat = jnp.concatenate([q * e[j] for j in range(NSUB)], axis=1).astype(BF16)
    kt = k * kscale
    km = jnp.concatenate([jnp.where(in_blk[j], kt, 0.0) for j in range(NSUB)], axis=1).astype(BF16)
    return dict(causal=causal, causal_t=causal_t, e=e, in_blk=in_blk, kscale=kscale, cend=cend, tail=tail,
                qcat=qcat, km=km)


def _chunk_fwd(q, k, v, g, st0, fw):
    t = _chunk_terms(q, k, g, fw)
    a = jnp.where(t["causal"], _dot_nt(t["qcat"], t["km"]), 0.0)
    o = _dot(a, v) + _dot_nt(t["qcat"][:, 0:HD], st0)
    st1 = st0 * jnp.exp(t["cend"]) + _dot_tn(v, k * t["tail"])
    return o, st1


def _chunk_bwd(q, k, v, g, st0, do, dst1, fw):
    t = _chunk_terms(q, k, g, fw)
    qcat, km, e = t["qcat"], t["km"], t["e"]
    a_t = jnp.where(t["causal_t"], _dot_nt(km, qcat), 0.0)
    ktail = k * t["tail"]
    dv = _dot(a_t, do) + _dot_nt(ktail, dst1)
    da = jnp.where(t["causal"], _dot_nt(do, v), 0.0)
    da_t = jnp.where(t["causal_t"], _dot_nt(v, do), 0.0)
    dqcat = _dot(da, km)
    dq_inter = e[0] * _dot(do, st0)
    dq = dq_inter
    for j in range(NSUB):
        dq = dq + e[j] * dqcat[:, j * HD:(j + 1) * HD]
    dkm = _dot(da_t, qcat)
    dkt = jnp.zeros_like(k)
    for j in range(NSUB):
        dkt = dkt + jnp.where(t["in_blk"][j], dkm[:, j * HD:(j + 1) * HD], 0.0)
    dk_inter = _dot(v, dst1) * t["tail"]
    dk = dkt * t["kscale"] + dk_inter
    dcum = q * dq_inter - k * dk_inter
    for j in range(NSUB):
        sl = slice(j * HD, (j + 1) * HD)
        dcum = dcum + qcat[:, sl].astype(F32) * dqcat[:, sl] - km[:, sl].astype(F32) * dkm[:, sl]
    ecend = jnp.exp(t["cend"])
    end = ecend * _colsum(st0 * dst1) + _colsum(k * dk_inter)
    dg = jnp.dot(t["causal_t"].astype(F32), dcum, precision=HIGHEST, preferred_element_type=F32) + end
    dst0 = dst1 * ecend + _dot_tn(do, q * e[0])
    return dq, dk, dv, dg, dst0


def _chunk_index(step, n_ctx_chunks, n_chunks, fw):
    if fw:
        return step
    return jnp.where(step < n_ctx_chunks, n_ctx_chunks - 1 - step, n_chunks - 1 + n_ctx_chunks - step)


def _hg_inputs(hq, hf, lbv, d_idx, sl):
    lb = _sigmoid(lbv[d_idx:d_idx + 1, sl] - lbv[2 + d_idx:3 + d_idx, sl])
    sg = _sigmoid(hf)
    f = lb + (1.0 - lb) * sg
    return _silu(hq), 1.0 - f, jnp.log(f), f, sg, lb


def _scan_fwd(p, side, n_ctx_chunks, fw, branch, name):
    rows = p.shape[0]
    n_chunks = rows // CHUNK
    d_idx = 0 if fw else 1
    hg = branch == "hg"
    cols = (C_HQ, C_HI, C_HF_FW + d_idx) if hg else (C_GQ, C_GK, C_GV)

    def body(*refs):
        if hg:
            a_ref, b_ref, c_ref, lb_ref, o_ref, st_ref, state = refs
        else:
            a_ref, b_ref, c_ref, lr_ref, wgk_ref, bgk_ref, o_ref, st_ref, state = refs
            logits = _dot(lr_ref[...], wgk_ref[...]) + bgk_ref[...]
            g_all = _log_sigmoid(logits) * (1.0 / GATE_NORM)

        @pl.when(pl.program_id(0) == 0)
        def _():
            state[...] = jnp.zeros_like(state)

        for h in range(NH):
            sl = slice(h * HD, (h + 1) * HD)
            if hg:
                q, k, g, _, _, _ = _hg_inputs(a_ref[:, sl], c_ref[:, sl], lb_ref[...], d_idx, sl)
                v = b_ref[:, sl]
            else:
                q, k, v, g = a_ref[:, sl] * (HD ** -0.5), b_ref[:, sl], c_ref[:, sl], g_all[:, sl]
            st0 = state[h]
            st_ref[0, h] = st0
            o, st1 = _chunk_fwd(q, k, v, g, st0, fw)
            o_ref[:, sl] = o
            state[h] = st1

    def cmap(blk):
        return pl.BlockSpec((CHUNK, HW), lambda j: (_chunk_index(j, n_ctx_chunks, n_chunks, fw), blk))

    fixed = lambda j: (0, 0)
    in_specs = [cmap(cols[0]), cmap(cols[1]), cmap(cols[2])]
    if hg:
        in_specs += [pl.BlockSpec((4, HW), fixed)]
        args = (p, p, p, side)
    else:
        in_specs += [pl.BlockSpec((CHUNK, 128), lambda j: (_chunk_index(j, n_ctx_chunks, n_chunks, fw), OFF_LR // 128)),
                     pl.BlockSpec((128, HW), fixed), pl.BlockSpec((1, HW), fixed)]
        args = (p, p, p, p, side[0], side[1])
    return pl.pallas_call(
        body, name=name, grid=(n_chunks,),
        out_shape=[jax.ShapeDtypeStruct((rows, HW), F32), jax.ShapeDtypeStruct((n_chunks, NH, HD, HD), F32)],
        in_specs=in_specs,
        out_specs=[pl.BlockSpec((CHUNK, HW), lambda j: (_chunk_index(j, n_ctx_chunks, n_chunks, fw), 0)),
                   pl.BlockSpec((1, NH, HD, HD), lambda j: (_chunk_index(j, n_ctx_chunks, n_chunks, fw), 0, 0, 0))],
        scratch_shapes=[pltpu.VMEM((NH, HD, HD), F32)],
        compiler_params=_cparams(dimension_semantics=("arbitrary",)),
    )(*args)


def _scan_bwd(p, side, states, d_o, n_ctx_chunks, fw, branch, name):
    rows = p.shape[0]
    n_chunks = rows // CHUNK
    d_idx = 0 if fw else 1
    hg = branch == "hg"
    cols = (C_HQ, C_HI, C_HF_FW + d_idx) if hg else (C_GQ, C_GK, C_GV)

    def body(*refs):
        if hg:
            a_ref, b_ref, c_ref, lb_ref, st_ref, do_ref, da_ref, db_ref, dc_ref, dlb_ref, dstate = refs
        else:
            (a_ref, b_ref, c_ref, lr_ref, wgk_ref, bgk_ref, st_ref, do_ref, da_ref, db_ref, dc_ref, dlr_ref,
             dwgk_ref, dbias_ref, dstate) = refs
            lr = lr_ref[...]
            logits = _dot(lr, wgk_ref[...]) + bgk_ref[...]
            g_all = _log_sigmoid(logits) * (1.0 / GATE_NORM)

        @pl.when(pl.program_id(0) == 0)
        def _():
            dstate[...] = jnp.zeros_like(dstate)
            if hg:
                dlb_ref[...] = jnp.zeros_like(dlb_ref)
            else:
                dwgk_ref[...] = jnp.zeros_like(dwgk_ref)
                dbias_ref[...] = jnp.zeros_like(dbias_ref)

        dg_parts = []
        for h in range(NH):
            sl = slice(h * HD, (h + 1) * HD)
            if hg:
                hq, hf = a_ref[:, sl], c_ref[:, sl]
                q, k, g, f, sg, lb = _hg_inputs(hq, hf, lb_ref[...], d_idx, sl)
                v = b_ref[:, sl]
            else:
                q, k, v, g = a_ref[:, sl] * (HD ** -0.5), b_ref[:, sl], c_ref[:, sl], g_all[:, sl]
            dq, dk, dv, dg, dst0 = _chunk_bwd(q, k, v, g, st_ref[0, h], do_ref[:, sl], dstate[h], fw)
            dstate[h] = dst0
            if hg:
                da_ref[:, sl] = dq * _dsilu(hq)
                db_ref[:, sl] = dv
                df = dg / f - dk
                dc_ref[:, sl] = df * (1.0 - lb) * sg * (1.0 - sg)
                dlb_ref[0:1, sl] += _colsum(df * (1.0 - sg))
            else:
                da_ref[:, sl] = dq * (HD ** -0.5)
                db_ref[:, sl] = dk
                dc_ref[:, sl] = dv
                dg_parts.append(dg)
        if not hg:
            dlogits = jnp.concatenate(dg_parts, axis=1) * (1.0 / GATE_NORM) * (1.0 - _sigmoid(logits))
            dlr_ref[...] = _dot_nt(dlogits, wgk_ref[...])
            dwgk_ref[...] += _dot_tn(lr, dlogits)
            dbias_ref[0:1, :] += _colsum(dlogits)

    def chunk_of(j):
        return _chunk_index(n_chunks - 1 - j, n_ctx_chunks, n_chunks, fw)

    def cmap(blk, width=HW):
        return pl.BlockSpec((CHUNK, width), lambda j: (chunk_of(j), blk))

    fixed = lambda j: (0, 0)
    st_spec = pl.BlockSpec((1, NH, HD, HD), lambda j: (chunk_of(j), 0, 0, 0))
    big = jax.ShapeDtypeStruct((rows, HW), F32)
    if hg:
        in_specs = [cmap(cols[0]), cmap(cols[1]), cmap(cols[2]), pl.BlockSpec((4, HW), fixed), st_spec, cmap(0)]
        args = (p, p, p, side, states, d_o)
        out_shape = [big, big, big, jax.ShapeDtypeStruct((8, HW), F32)]
        out_specs = [cmap(0), cmap(0), cmap(0), pl.BlockSpec((8, HW), fixed)]
    else:
        in_specs = [cmap(cols[0]), cmap(cols[1]), cmap(cols[2]), cmap(OFF_LR // 128, 128),
                    pl.BlockSpec((128, HW), fixed), pl.BlockSpec((1, HW), fixed), st_spec, cmap(0)]
        args = (p, p, p, p, side[0], side[1], states, d_o)
        out_shape = [big, big, big, jax.ShapeDtypeStruct((rows, 128), F32), jax.ShapeDtypeStruct((128, HW), F32),
                     jax.ShapeDtypeStruct((8, HW), F32)]
        out_specs = [cmap(0), cmap(0), cmap(0), cmap(0, 128), pl.BlockSpec((128, HW), fixed),
                     pl.BlockSpec((8, HW), fixed)]
    return pl.pallas_call(
        body, name=name, grid=(n_chunks,), out_shape=out_shape, in_specs=in_specs, out_specs=out_specs,
        scratch_shapes=[pltpu.VMEM((NH, HD, HD), F32)],
        compiler_params=_cparams(dimension_semantics=("arbitrary",)),
    )(*args)


SMALL_ROWS = 32


def _reduce_small(gathered, lb_full, name):
    _, _, d = gathered.shape

    def body(g_ref, lb_ref, sum_ref, dmod_ref, dbmod_ref, dlb_ref):
        total = g_ref[0]
        for b in range(1, N_DEV):
            total = total + g_ref[b]
        sum_ref[...] = total
        dmod_ref[...] = jnp.zeros_like(dmod_ref)
        for m in range(N_MOD):
            col = slice(m * d, (m + 1) * d)
            acc = jnp.zeros((1, d), F32)
            for b in range(N_DEV):
                row = g_ref[b, m:m + 1, :]
                dmod_ref[b:b + 1, col] = row
                acc = acc + row
            if m < 2:
                ctx_row = total[6 + m:7 + m, :]
                dmod_ref[8:9, col] = ctx_row
                acc = acc + ctx_row
            dbmod_ref[:, col] = acc
        lbv = lb_ref[...]
        for dd in range(2):
            lb = _sigmoid(lbv[dd:dd + 1, :] - lbv[2 + dd:3 + dd, :])
            gl = total[13:14, dd * HW:(dd + 1) * HW] * lb * (1.0 - lb)
            dlb_ref[dd:dd + 1, :] = gl
            dlb_ref[2 + dd:3 + dd, :] = -gl

    return pl.pallas_call(
        body, name=name,
        out_shape=[jax.ShapeDtypeStruct((SMALL_ROWS, d), F32), jax.ShapeDtypeStruct((16, N_MOD * d), F32),
                   jax.ShapeDtypeStruct((1, N_MOD * d), F32), jax.ShapeDtypeStruct((4, HW), F32)],
        in_specs=[VMEM_SPEC] * 2, out_specs=[VMEM_SPEC] * 4, compiler_params=_cparams(),
    )(gathered, lb_full)


def _c_ctx_grad(gathered, c_ctx_row, name):
    def body(g_ref, c_ref, o_ref):
        acc = g_ref[0, 0:1, :]
        for chip in range(1, N_CHIP):
            acc = acc + g_ref[2 * chip, 0:1, :]
        o_ref[...] = acc * _dsilu(c_ref[...])

    return pl.pallas_call(
        body, name=name, out_shape=jax.ShapeDtypeStruct(c_ctx_row.shape, F32),
        in_specs=[VMEM_SPEC] * 2, out_specs=VMEM_SPEC, compiler_params=_cparams(),
    )(gathered, c_ctx_row)


def _relayout_w_in(w):
    pad = jnp.zeros((w.shape[0], 128 - 2 * RANK), w.dtype)
    return jnp.concatenate([w[:, :9 * HW], w[:, 9 * HW + 2 * RANK:], w[:, 9 * HW:9 * HW + 2 * RANK], pad], axis=1)


def _unrelayout_w_in(g_main, g_lr):
    return jnp.concatenate([g_main[:, :9 * HW], g_lr[:, :2 * RANK], g_main[:, 9 * HW:]], axis=1)


def _blocked(full, n_blocks):
    k, n = full.shape
    return full.reshape(k, n_blocks, n // n_blocks).transpose(1, 0, 2)


def _unblocked(blocks):
    nb, k, n = blocks.shape
    return blocks.transpose(1, 0, 2).reshape(k, nb * n)


def _sample_step(x0, ctx0, target0, modc, modx, norm_pre1, norms, onorms, lb_full, gla_side, w_in_r, wbh, wbg, wout,
                 wg, wu, wd):
    seq, d = x0.shape
    ctx_len = ctx0.shape[0]
    n_ctx_tiles = ctx_len // TM
    n_tiles = (ctx_len + seq) // TM
    n_ctx_chunks = ctx_len // CHUNK
    z = jnp.concatenate([ctx0, x0], axis=0)
    h1, p = _in_projection(z, modc, modx, norm_pre1, w_in_r, n_ctx_tiles, "in_projection")
    o_hg_fw, st_hg_fw = _scan_fwd(p, lb_full, n_ctx_chunks, True, "hg", "scan_hg_fw")
    o_hg_bw, st_hg_bw = _scan_fwd(p, lb_full, n_ctx_chunks, False, "hg", "scan_hg_bw")
    o_gla_fw, st_gla_fw = _scan_fwd(p, gla_side[0], n_ctx_chunks, True, "gla", "scan_gla_fw")
    o_gla_bw, st_gla_bw = _scan_fwd(p, gla_side[1], n_ctx_chunks, False, "gla", "scan_gla_bw")
    o_list = [o_hg_fw, o_hg_bw, o_gla_fw, o_gla_bw]
    (loss_part, dz2, y1, merged, og_hg, og_gla, h2, a_act, du, dv, dy2, stat_ffn) = _mixer_ffn(
        x0, p, o_list, modx, norms, onorms, wbh, wbg, wout, wg, wu, wd, target0, n_ctx_tiles, "mixer_ffn")

    (d_ohg, d_ogla, d_hgate, d_ggate, d_ghg, d_ggla, dy1, db_hg, db_gla, stat_mix) = _mixer_tail_bwd(
        x0, p, o_list, dz2, y1, modx, norms, onorms, wbh, wbg, wout, n_ctx_tiles, n_tiles, "mixer_tail_bwd")
    dhq_f, dhi_f, dhf_f, dlb_f = _scan_bwd(p, lb_full, st_hg_fw, d_ohg, n_ctx_chunks, True, "hg", "scan_hg_fw_bwd")
    dhq_b, dhi_b, dhf_b, dlb_b = _scan_bwd(p, lb_full, st_hg_bw, d_ohg, n_ctx_chunks, False, "hg", "scan_hg_bw_bwd")
    dgq_f, dgk_f, dgv_f, dlr_f, dwgk_f, dbgk_f = _scan_bwd(p, gla_side[0], st_gla_fw, d_ogla, n_ctx_chunks, True, "gla",
                                                           "scan_gla_fw_bwd")
    dgq_b, dgk_b, dgv_b, dlr_b, dwgk_b, dbgk_b = _scan_bwd(p, gla_side[1], st_gla_bw, d_ogla, n_ctx_chunks, False, "gla",
                                                           "scan_gla_bw_bwd")
    pieces = [dhq_f, dhq_b, dhi_f, dhi_b, dhf_f, dhf_b, d_hgate, dgq_f, dgq_b, dgk_f, dgk_b, dgv_f, dgv_b, d_ggate,
              d_ghg, d_ggla, dlr_f, dlr_b]
    dp, grad_x, stat_in = _in_projection_bwd(z, dz2, modc, modx, norm_pre1, w_in_r, pieces, n_ctx_tiles,
                                             "in_projection_bwd")

    g_in_main = _weight_grad_cols(h1, dp, OFF_LR, "grad_w_in_main")
    g_in_lr = _weight_grad(h1, dp, "grad_w_in_lr", col_block=(128, OFF_LR // 128))
    dff = wg.shape[1]
    tn_ff = dff // N_CHIP if (dff // N_CHIP) % 128 == 0 else 256
    return dict(
        loss_part=loss_part, grad_x=grad_x, stat_in=stat_in, stat_mix=stat_mix, stat_ffn=stat_ffn,
        dlb=(dlb_f, dlb_b), dwgk=(dwgk_f, dwgk_b), dbgk=(dbgk_f, dbgk_b),
        g_w_in=_unrelayout_w_in(g_in_main, g_in_lr),
        g_br_hg=_weight_grad(og_hg, db_hg, "grad_w_br_hg"),
        g_br_gla=_weight_grad(og_gla, db_gla, "grad_w_br_gla"),
        g_out=_weight_grad(merged, dy1, "grad_w_out"),
        g_gate=_weight_grad(h2, du, "grad_w_ff_gate", tn=tn_ff),
        g_up=_weight_grad(h2, dv, "grad_w_ff_up", tn=tn_ff),
        g_down=_weight_grad(a_act, dy2, "grad_w_ff_down", tk=tn_ff))


def kernel(x, c, ctx, c_ctx, w_mod, b_mod, norm_pre1, norm_post1, norm_pre2, norm_post2, w_in, hg_lb, hg_onorm, gla_w_gk, gla_b_gk, gla_onorm, w_br_hg, w_br_gla, w_out, w_ff_gate, w_ff_up, w_ff_down, loss_target, m_c_ctx, m_w_mod, m_b_mod, m_norm_pre1, m_norm_post1, m_norm_pre2, m_norm_post2, m_w_in, m_hg_lb, m_hg_onorm, m_gla_w_gk, m_gla_b_gk, m_gla_onorm, m_w_br_hg, m_w_br_gla, m_w_out, m_w_ff_gate, m_w_ff_up, m_w_ff_down, v_c_ctx, v_w_mod, v_b_mod, v_norm_pre1, v_norm_post1, v_norm_pre2, v_norm_post2, v_w_in, v_hg_lb, v_hg_onorm, v_gla_w_gk, v_gla_b_gk, v_gla_onorm, v_w_br_hg, v_w_br_gla, v_w_out, v_w_ff_gate, v_w_ff_up, v_w_ff_down):
    seq, d = x.shape[1], x.shape[2]
    ctx_len = ctx.shape[1]
    assert seq % TM == 0 and ctx_len % TM == 0 and d == 2 * HW
    ax, ay, ac = lax.axis_index("x"), lax.axis_index("y"), lax.axis_index("c")
    chip = 2 * ax + ay
    dev = 2 * chip + ac
    c_arr = jnp.reshape(ac, (1,)).astype(jnp.int32)

    small1 = jnp.concatenate([
        c.reshape(d // 128, 128), hg_lb.reshape(4, 128), gla_w_gk.reshape(2 * RANK, 128), gla_b_gk.reshape(2, 128),
        jnp.zeros((48 - d // 128 - 4 - 2 * RANK - 2, 128), F32)], axis=0)
    got1 = _allgather8(small1, "gather_small_params")
    nc = d // 128
    c_all = got1[:, :nc, :].reshape(N_DEV, d)
    per_chip = got1[0::2]
    lb_full = per_chip[:, nc:nc + 4, :].transpose(1, 0, 2).reshape(4, HW)
    wgk_full = per_chip[:, nc + 4:nc + 4 + 2 * RANK, :].transpose(1, 0, 2).reshape(2, RANK, HW)
    bgk_full = per_chip[:, nc + 4 + 2 * RANK:nc + 6 + 2 * RANK, :].transpose(1, 0, 2).reshape(2, HW)
    wgk_pad = [jnp.zeros((128, HW), F32).at[dd * RANK:(dd + 1) * RANK].set(wgk_full[dd]) for dd in range(2)]
    bgk = [bgk_full[dd:dd + 1] for dd in range(2)]

    n_mod_cols = w_mod.shape[2]
    cond = jnp.concatenate([c_all, c_ctx.reshape(1, d), jnp.zeros((7, d), F32)], axis=0)
    b_cols = lax.dynamic_slice(b_mod, (0, chip * n_mod_cols), (1, n_mod_cols))
    mod_part = _mod_forward(cond, w_mod[0], b_cols, "mod_forward")
    mod_got = _allgather8(mod_part, "gather_mod")
    mod_all = mod_got[0::2].transpose(1, 0, 2).reshape(16, N_CHIP * n_mod_cols)
    pad2 = jnp.zeros((2, d), F32)
    modx = jnp.concatenate([lax.dynamic_slice(mod_all, (dev, 0), (1, N_MOD * d)).reshape(N_MOD, d), pad2], axis=0)
    modc = jnp.concatenate([mod_all[8].reshape(N_MOD, d), pad2], axis=0)

    shards = [_cast_bf16(w_in[0], "cast_w_in"), _cast_bf16(w_br_hg[0], "cast_w_br_hg"),
              _cast_bf16(w_br_gla[0], "cast_w_br_gla"), _cast_bf16(w_out[0], "cast_w_out"),
              _cast_bf16(w_ff_gate[0], "cast_w_ff_gate"), _cast_bf16(w_ff_up[0], "cast_w_ff_up"),
              _cast_bf16(w_ff_down[0], "cast_w_ff_down")]
    gathered = _gather_shards(shards, "gather_weights")
    w_in_r = _relayout_w_in(_unblocked(gathered[0]))
    wbh, wbg = _unblocked(gathered[1]), _unblocked(gathered[2])
    wout = gathered[3].reshape(d, d)
    wg, wu = _unblocked(gathered[4]), _unblocked(gathered[5])
    wd = gathered[6].reshape(wg.shape[1], d)

    norms = jnp.concatenate([norm_pre1, norm_post1, norm_pre2, norm_post2, jnp.zeros((4, d), F32)], axis=0)
    onorms = jnp.zeros((8, d), F32).at[0, :HD].set(hg_onorm[0]).at[1, :HD].set(gla_onorm[0])
    gla_side = [(wgk_pad[dd], bgk[dd]) for dd in range(2)]
    r = _sample_step(x[0], ctx[0], loss_target[0], modc, modx, norm_pre1, norms, onorms, lb_full, gla_side, w_in_r,
                     wbh, wbg, wout, wg, wu, wd)
    loss_part, grad_x, stat_in, stat_mix, stat_ffn = (r[k] for k in ("loss_part", "grad_x", "stat_in", "stat_mix",
                                                                     "stat_ffn"))
    (dlb_f, dlb_b), (dwgk_f, dwgk_b), (dbgk_f, dbgk_b) = r["dlb"], r["dwgk"], r["dbgk"]
    dff = wg.shape[1]

    full = [_blocked(r["g_w_in"], N_CHIP), _blocked(r["g_br_hg"], N_CHIP), _blocked(r["g_br_gla"], N_CHIP),
            r["g_out"].reshape(N_CHIP, d // N_CHIP, d), _blocked(r["g_gate"], N_CHIP), _blocked(r["g_up"], N_CHIP),
            r["g_down"].reshape(N_CHIP, dff // N_CHIP, d)]
    from_sibling = _send_other_half(full, "grads_to_sibling")
    names = ["w_in", "w_br_hg", "w_br_gla", "w_out", "w_ff_gate", "w_ff_up", "w_ff_down"]
    pairs = [_pair_sum(c_arr, f, r, "pair_sum_" + nm) for f, r, nm in zip(full, from_sibling, names)]
    at_owner = _blocks_to_owner(pairs, "grads_to_owner")
    own_half = [_sum_chips(g, "chip_sum_" + nm) for g, nm in zip(at_owner, names)]
    other_half = _swap_with_sibling(own_half, "halves_to_sibling")
    big_w = [w_in, w_br_hg, w_br_gla, w_out, w_ff_gate, w_ff_up, w_ff_down]
    big_m = [m_w_in, m_w_br_hg, m_w_br_gla, m_w_out, m_w_ff_gate, m_w_ff_up, m_w_ff_down]
    big_v = [v_w_in, v_w_br_hg, v_w_br_gla, v_w_out, v_w_ff_gate, v_w_ff_up, v_w_ff_down]
    big = {}
    for nm, own, oth, w_, m_, v_ in zip(names, own_half, other_half, big_w, big_m, big_v):
        res = _adamw_halves(c_arr, own, oth, w_[0], m_[0], v_[0], "adamw_" + nm)
        big[nm] = [r[None] for r in res]

    zrow = jnp.zeros((1, d), F32)
    small2 = jnp.concatenate([
        stat_in[0:2], stat_mix[0:1], stat_ffn[0:3],
        stat_in[2:4],
        stat_in[4:5], stat_mix[1:2], stat_ffn[3:4], stat_ffn[4:5],
        stat_mix[2:3],
        jnp.concatenate([dlb_f[0:1], dlb_b[0:1]], axis=1),
        jnp.concatenate([dbgk_f[0:1], dbgk_b[0:1]], axis=1),
        zrow,
        jnp.concatenate([dwgk_f[0:RANK], dwgk_b[RANK:2 * RANK]], axis=1),
    ], axis=0)
    got2 = _allgather8(small2, "gather_small_grads")
    total, dmod_all, g_b_mod, g_lb_full = _reduce_small(got2, lb_full, "reduce_small")
    dmod_cols = lax.dynamic_slice(dmod_all, (0, chip * n_mod_cols), (16, n_mod_cols))
    g_w_mod, cctx_part = _mod_backward(cond, w_mod[0], dmod_cols, "mod_backward")
    got3 = _allgather8(cctx_part, "gather_c_ctx_grad")
    g_c_ctx = _c_ctx_grad(got3, c_ctx.reshape(1, d), "c_ctx_grad")

    g_pre1, g_post1, g_pre2, g_post2 = total[8:9], total[9:10], total[10:11], total[11:12]
    g_hg_on, g_gla_on = total[12:13, 0:HD], total[12:13, HD:2 * HD]
    n_lb = hg_lb.shape[2]
    g_hg_lb = lax.dynamic_slice(g_lb_full, (0, chip * n_lb), (4, n_lb))
    g_bgk = lax.dynamic_slice(total[14:15].reshape(2, HW), (0, chip * n_lb), (2, n_lb))
    g_wgk_full = total[16:16 + RANK].reshape(RANK, 2, HW).transpose(1, 0, 2).reshape(2 * RANK, HW)
    g_wgk = lax.dynamic_slice(g_wgk_full, (0, chip * n_lb), (2 * RANK, n_lb))

    small_items = [
        (g_c_ctx, c_ctx.reshape(1, d), m_c_ctx.reshape(1, d), v_c_ctx.reshape(1, d)),
        (g_b_mod, b_mod, m_b_mod, v_b_mod),
        (g_pre1, norm_pre1, m_norm_pre1, v_norm_pre1),
        (g_post1, norm_post1, m_norm_post1, v_norm_post1),
        (g_pre2, norm_pre2, m_norm_pre2, v_norm_pre2),
        (g_post2, norm_post2, m_norm_post2, v_norm_post2),
        (g_hg_lb, hg_lb.reshape(4, n_lb), m_hg_lb.reshape(4, n_lb), v_hg_lb.reshape(4, n_lb)),
        (g_hg_on, hg_onorm, m_hg_onorm, v_hg_onorm),
        (g_wgk, gla_w_gk.reshape(2 * RANK, n_lb), m_gla_w_gk.reshape(2 * RANK, n_lb), v_gla_w_gk.reshape(2 * RANK, n_lb)),
        (g_bgk, gla_b_gk.reshape(2, n_lb), m_gla_b_gk.reshape(2, n_lb), v_gla_b_gk.reshape(2, n_lb)),
        (g_gla_on, gla_onorm, m_gla_onorm, v_gla_onorm),
    ]
    small_res = _adamw_whole(small_items, "adamw_small")
    mod_res = _adamw_tiled(g_w_mod, w_mod[0], m_w_mod[0], v_w_mod[0], "adamw_w_mod")

    loss = lax.psum(loss_part[0, 0], ("x", "y", "c"))

    shapes = dict(c_ctx=c_ctx.shape, b_mod=b_mod.shape, norm_pre1=norm_pre1.shape, norm_post1=norm_post1.shape,
                  norm_pre2=norm_pre2.shape, norm_post2=norm_post2.shape, hg_lb=hg_lb.shape, hg_onorm=hg_onorm.shape,
                  gla_w_gk=gla_w_gk.shape, gla_b_gk=gla_b_gk.shape, gla_onorm=gla_onorm.shape)
    small_names = ["c_ctx", "b_mod", "norm_pre1", "norm_post1", "norm_pre2", "norm_post2", "hg_lb", "hg_onorm",
                   "gla_w_gk", "gla_b_gk", "gla_onorm"]
    grads, deltas, new_m, new_v = {}, {}, {}, {}
    for nm, item, res in zip(small_names, small_items, small_res):
        grads[nm] = item[0].reshape(shapes[nm])
        deltas[nm], new_m[nm], new_v[nm] = (r.reshape(shapes[nm]) for r in res)
    grads["w_mod"] = g_w_mod[None]
    deltas["w_mod"], new_m["w_mod"], new_v["w_mod"] = (r[None] for r in mod_res)
    for nm in names:
        grads[nm], deltas[nm], new_m[nm], new_v[nm] = big[nm]
    order = ["c_ctx", "w_mod", "b_mod", "norm_pre1", "norm_post1", "norm_pre2", "norm_post2", "w_in", "hg_lb",
             "hg_onorm", "gla_w_gk", "gla_b_gk", "gla_onorm", "w_br_hg", "w_br_gla", "w_out", "w_ff_gate", "w_ff_up",
             "w_ff_down"]
    return (loss, grad_x[None], *[grads[n] for n in order], *[deltas[n] for n in order],
            *[new_m[n] for n in order], *[new_v[n] for n in order])


def _weight_grad_cols(xs, dy, n_cols, name, tk=256, tn=512):
    rows = dy.shape[0]
    k = xs.shape[1]

    def body(x_ref, dy_ref, o_ref):
        o_ref[...] = _dot_tn(x_ref[...], dy_ref[...])

    return pl.pallas_call(
        body, name=name, grid=(k // tk, n_cols // tn),
        out_shape=jax.ShapeDtypeStruct((k, n_cols), F32),
        in_specs=[pl.BlockSpec((rows, tk), lambda i, j: (0, i)), pl.BlockSpec((rows, tn), lambda i, j: (0, j))],
        out_specs=pl.BlockSpec((tk, tn), lambda i, j: (i, j)),
        compiler_params=_cparams(dimension_semantics=("parallel", "parallel")),
    )(xs, dy)
```

```python
import functools

import jax
import jax.numpy as jnp
from jax import lax
from jax.experimental import pallas as pl
from jax.experimental.pallas import tpu as pltpu

F32 = jnp.float32
BF16 = jnp.bfloat16
HIGHEST = lax.Precision.HIGHEST
MESH = pl.DeviceIdType.MESH

EPS = 1e-6
CHUNK = 64
SUB = 16
NSUB = CHUNK // SUB
NH = 4
HD = 128
HW = NH * HD
RANK = 16
GATE_NORM = 16.0
N_MOD = 6
TM = 256
TM_FFN = 128
N_DEV = 8
N_CHIP = 4
VMEM_LIMIT = 56 * 1024 * 1024

ADAM_LR = 0.001
ADAM_B1 = 0.9
ADAM_B2 = 0.999
ADAM_EPS = 1e-08
ADAM_WD = 0.01
ADAM_STEP = 10

VMEM_SPEC = pl.BlockSpec(memory_space=pltpu.VMEM)
ANY_SPEC = pl.BlockSpec(memory_space=pl.ANY)


def _cparams(**kw):
    return pltpu.CompilerParams(vmem_limit_bytes=VMEM_LIMIT, **kw)


def _dot(a, b):
    return jnp.dot(a.astype(BF16), b.astype(BF16), preferred_element_type=F32)


def _dot_nt(a, b):
    return lax.dot_general(a.astype(BF16), b.astype(BF16), (((1,), (1,)), ((), ())), preferred_element_type=F32)


def _dot_tn(a, b):
    return lax.dot_general(a.astype(BF16), b.astype(BF16), (((0,), (0,)), ((), ())), preferred_element_type=F32)


def _sigmoid(x):
    return 1.0 / (1.0 + jnp.exp(-x))


def _silu(x):
    return x * _sigmoid(x)


def _dsilu(x):
    s = _sigmoid(x)
    return s * (1.0 + x * (1.0 - s))


def _log_sigmoid(x):
    return jnp.minimum(x, 0.0) - jnp.log(1.0 + jnp.exp(-jnp.abs(x)))


def _colsum(a):
    return jnp.sum(a, axis=0, keepdims=True)


def _rms(a):
    r = lax.rsqrt(jnp.mean(a * a, axis=-1, keepdims=True) + EPS)
    return a * r, r


def _rms_bwd(dn, n, r):
    return r * (dn - n * jnp.mean(dn * n, axis=-1, keepdims=True))


def _place():
    x, y, c = lax.axis_index("x"), lax.axis_index("y"), lax.axis_index("c")
    chips = [(1 - x, y), (x, 1 - y), (1 - x, 1 - y)]
    return x, y, c, chips


def _allgather8(v, name):
    rows, cols = v.shape

    def body(x_ref, out_ref, send_sems, recv_sems, local_sem):
        x, y, c, chips = _place()
        me, sibling = (x, y, c), (x, y, 1 - c)

        def blk(px, py, pc):
            return out_ref.at[4 * px + 2 * py + pc]

        def copy(k, block, to, src=None):
            return pltpu.make_async_remote_copy(
                src_ref=blk(*block) if src is None else src, dst_ref=blk(*block),
                send_sem=send_sems.at[k], recv_sem=recv_sems.at[k], device_id=to, device_id_type=MESH)

        mine = pltpu.make_async_copy(x_ref, blk(*me), local_sem)
        mine.start()
        first = [copy(0, me, sibling, src=x_ref)]
        first += [copy(1 + j, me, (*chip, c), src=x_ref) for j, chip in enumerate(chips)]
        for cp in first:
            cp.start()
        passed = [copy(4 + j, (*chip, c), sibling) for j, chip in enumerate(chips)]
        for j, chip in enumerate(chips):
            copy(1 + j, (*chip, c), me).wait_recv()
            passed[j].start()
        copy(0, sibling, me).wait_recv()
        for j, chip in enumerate(chips):
            copy(4 + j, (*chip, 1 - c), me).wait_recv()
        for cp in first + passed:
            cp.wait_send()
        mine.wait()

    return pl.pallas_call(
        body, name=name,
        out_shape=jax.ShapeDtypeStruct((N_DEV, rows, cols), v.dtype),
        in_specs=[VMEM_SPEC], out_specs=VMEM_SPEC,
        scratch_shapes=[pltpu.SemaphoreType.DMA((7,)), pltpu.SemaphoreType.DMA((7,)), pltpu.SemaphoreType.DMA],
    )(v)


def _gather_shards(shards, name):
    n = len(shards)

    def body(*refs):
        ins, outs = refs[:n], refs[n:2 * n]
        send_sems, recv_sems, local_sems = refs[2 * n:]
        x, y, c, chips = _place()
        me_chip = 2 * x + y
        sibling = (x, y, 1 - c)

        def half(k, chip_id, pc):
            h = shards[k].shape[0] // 2
            return outs[k].at[chip_id, pl.ds(pl.multiple_of(pc * h, 8), h), :]

        def copy(k, j, chip_id, pc, to, src=None):
            return pltpu.make_async_remote_copy(
                src_ref=half(k, chip_id, pc) if src is None else src, dst_ref=half(k, chip_id, pc),
                send_sem=send_sems.at[k, j], recv_sem=recv_sems.at[k, j], device_id=to, device_id_type=MESH)

        locals_ = []
        for k in range(n):
            cp = pltpu.make_async_copy(ins[k], outs[k].at[me_chip], local_sems.at[k])
            cp.start()
            locals_.append(cp)
        started = []
        for k in range(n):
            h = shards[k].shape[0] // 2
            src = ins[k].at[pl.ds(pl.multiple_of(c * h, 8), h), :]
            for j, chip in enumerate(chips):
                cp = copy(k, j, me_chip, c, (*chip, c), src=src)
                cp.start()
                started.append(cp)
        for k in range(n):
            for j, (px, py) in enumerate(chips):
                copy(k, j, 2 * px + py, c, sibling).wait_recv()
                cp = copy(k, 3 + j, 2 * px + py, c, sibling)
                cp.start()
                started.append(cp)
        for k in range(n):
            for j, (px, py) in enumerate(chips):
                copy(k, 3 + j, 2 * px + py, 1 - c, sibling).wait_recv()
        for cp in started:
            cp.wait_send()
        for cp in locals_:
            cp.wait()

    return pl.pallas_call(
        body, name=name,
        out_shape=[jax.ShapeDtypeStruct((N_CHIP,) + s.shape, s.dtype) for s in shards],
        in_specs=[ANY_SPEC] * n, out_specs=[ANY_SPEC] * n,
        scratch_shapes=[pltpu.SemaphoreType.DMA((n, 6)), pltpu.SemaphoreType.DMA((n, 6)),
                        pltpu.SemaphoreType.DMA((n,))],
    )(*shards)


def _send_other_half(arrs, name):
    n = len(arrs)

    def body(*refs):
        ins, outs = refs[:n], refs[n:2 * n]
        send_sems, recv_sems = refs[2 * n:]
        x, y, c, _ = _place()
        cps = []
        for k in range(n):
            h = arrs[k].shape[1] // 2
            cp = pltpu.make_async_remote_copy(
                src_ref=ins[k].at[:, pl.ds(pl.multiple_of((1 - c) * h, 8), h), :], dst_ref=outs[k],
                send_sem=send_sems.at[k], recv_sem=recv_sems.at[k], device_id=(x, y, 1 - c), device_id_type=MESH)
            cp.start()
            cps.append(cp)
        for cp in cps:
            cp.wait()

    return pl.pallas_call(
        body, name=name,
        out_shape=[jax.ShapeDtypeStruct((a.shape[0], a.shape[1] // 2, a.shape[2]), a.dtype) for a in arrs],
        in_specs=[ANY_SPEC] * n, out_specs=[ANY_SPEC] * n,
        scratch_shapes=[pltpu.SemaphoreType.DMA((n,)), pltpu.SemaphoreType.DMA((n,))],
    )(*arrs)


def _blocks_to_owner(arrs, name):
    n = len(arrs)

    def body(*refs):
        ins, outs = refs[:n], refs[n:2 * n]
        send_sems, recv_sems, local_sems = refs[2 * n:]
        x, y, c, chips = _place()
        me_chip = 2 * x + y
        locals_, started = [], []
        for k in range(n):
            cp = pltpu.make_async_copy(ins[k].at[me_chip], outs[k].at[me_chip], local_sems.at[k])
            cp.start()
            locals_.append(cp)

        def copy(k, j, src_block, dst_slot, to):
            return pltpu.make_async_remote_copy(
                src_ref=ins[k].at[src_block], dst_ref=outs[k].at[dst_slot],
                send_sem=send_sems.at[k, j], recv_sem=recv_sems.at[k, j], device_id=to, device_id_type=MESH)

        for k in range(n):
            for j, (px, py) in enumerate(chips):
                cp = copy(k, j, 2 * px + py, me_chip, (px, py, c))
                cp.start()
                started.append(cp)
        for k in range(n):
            for j, (px, py) in enumerate(chips):
                copy(k, j, me_chip, 2 * px + py, (px, py, c)).wait_recv()
        for cp in started:
            cp.wait_send()
        for cp in locals_:
            cp.wait()

    return pl.pallas_call(
        body, name=name,
        out_shape=[jax.ShapeDtypeStruct(a.shape, a.dtype) for a in arrs],
        in_specs=[ANY_SPEC] * n, out_specs=[ANY_SPEC] * n,
        scratch_shapes=[pltpu.SemaphoreType.DMA((n, 3)), pltpu.SemaphoreType.DMA((n, 3)),
                        pltpu.SemaphoreType.DMA((n,))],
    )(*arrs)


def _swap_with_sibling(arrs, name):
    n = len(arrs)

    def body(*refs):
        ins, outs = refs[:n], refs[n:2 * n]
        send_sems, recv_sems = refs[2 * n:]
        x, y, c, _ = _place()
        cps = []
        for k in range(n):
            cp = pltpu.make_async_remote_copy(
                src_ref=ins[k], dst_ref=outs[k], send_sem=send_sems.at[k], recv_sem=recv_sems.at[k],
                device_id=(x, y, 1 - c), device_id_type=MESH)
            cp.start()
            cps.append(cp)
        for cp in cps:
            cp.wait()

    return pl.pallas_call(
        body, name=name,
        out_shape=[jax.ShapeDtypeStruct(a.shape, a.dtype) for a in arrs],
        in_specs=[ANY_SPEC] * n, out_specs=[ANY_SPEC] * n,
        scratch_shapes=[pltpu.SemaphoreType.DMA((n,)), pltpu.SemaphoreType.DMA((n,))],
    )(*arrs)


def _row_tile(h, mult=8, cap=128):
    for t in range(cap - cap % mult, mult - 1, -mult):
        if h % t == 0:
            return t
    raise ValueError(h)


def _cast_bf16(a, name):
    rows, cols = a.shape
    tr = _row_tile(rows, 16, 256)

    def body(a_ref, o_ref):
        o_ref[...] = a_ref[...].astype(BF16)

    return pl.pallas_call(
        body, name=name, grid=(rows // tr,),
        out_shape=jax.ShapeDtypeStruct(a.shape, BF16),
        in_specs=[pl.BlockSpec((tr, cols), lambda i: (i, 0))],
        out_specs=pl.BlockSpec((tr, cols), lambda i: (i, 0)),
        compiler_params=_cparams(dimension_semantics=("parallel",)),
    )(a)


def _pair_sum(c_arr, full, recv, name):
    nb, rows, cols = full.shape
    h = rows // 2
    tr = _row_tile(h, 16, 256)
    steps = h // tr

    def body(c_ref, f_ref, r_ref, o_ref):
        o_ref[...] = (f_ref[...] + r_ref[...]).astype(BF16)

    return pl.pallas_call(
        body, name=name,
        grid_spec=pltpu.PrefetchScalarGridSpec(
            num_scalar_prefetch=1, grid=(nb, steps),
            in_specs=[pl.BlockSpec((1, tr, cols), lambda b, i, c_ref: (b, c_ref[0] * steps + i, 0)),
                      pl.BlockSpec((1, tr, cols), lambda b, i, c_ref: (b, i, 0))],
            out_specs=pl.BlockSpec((1, tr, cols), lambda b, i, c_ref: (b, i, 0))),
        out_shape=jax.ShapeDtypeStruct((nb, h, cols), BF16),
        compiler_params=_cparams(dimension_semantics=("parallel", "parallel")),
    )(c_arr, full, recv)


def _sum_chips(got, name):
    nb, h, cols = got.shape
    tr = _row_tile(h, 16, 256)

    def body(g_ref, o_ref):
        g = g_ref[...].astype(F32)
        o_ref[...] = ((g[0] + g[1]) + g[2]) + g[3]

    return pl.pallas_call(
        body, name=name, grid=(h // tr,),
        out_shape=jax.ShapeDtypeStruct((h, cols), F32),
        in_specs=[pl.BlockSpec((nb, tr, cols), lambda i: (0, i, 0))],
        out_specs=pl.BlockSpec((tr, cols), lambda i: (i, 0)),
        compiler_params=_cparams(dimension_semantics=("parallel",)),
    )(got)


def _adam_math(g, w, m, v):
    m1 = ADAM_B1 * m + (1.0 - ADAM_B1) * g
    v1 = ADAM_B2 * v + (1.0 - ADAM_B2) * (g * g)
    m_hat = m1 / (1.0 - ADAM_B1 ** ADAM_STEP)
    v_hat = v1 / (1.0 - ADAM_B2 ** ADAM_STEP)
    delta = -ADAM_LR * (m_hat / (jnp.sqrt(v_hat) + ADAM_EPS) + ADAM_WD * w)
    return delta, m1, v1


def _adamw_halves(c_arr, own, other, w, m, v, name):
    rows, cols = w.shape
    h = rows // 2
    tr = _row_tile(h)
    steps = h // tr

    def body(c_ref, own_ref, oth_ref, w_ref, m_ref, v_ref, g_out, d_out, m_out, v_out):
        g = jnp.where(pl.program_id(0) == c_ref[0], own_ref[...], oth_ref[...])
        d, m1, v1 = _adam_math(g, w_ref[...], m_ref[...], v_ref[...])
        g_out[...] = g
        d_out[...] = d
        m_out[...] = m1
        v_out[...] = v1

    half_spec = pl.BlockSpec((tr, cols), lambda p, i, c_ref: (i, 0))
    full_spec = pl.BlockSpec((tr, cols), lambda p, i, c_ref: (p * steps + i, 0))
    return pl.pallas_call(
        body, name=name,
        grid_spec=pltpu.PrefetchScalarGridSpec(
            num_scalar_prefetch=1, grid=(2, steps),
            in_specs=[half_spec, half_spec, full_spec, full_spec, full_spec],
            out_specs=[full_spec] * 4),
        out_shape=[jax.ShapeDtypeStruct(w.shape, F32)] * 4,
        compiler_params=_cparams(dimension_semantics=("parallel", "parallel")),
    )(c_arr, own, other, w, m, v)


def _adamw_whole(items, name):
    n = len(items)

    def body(*refs):
        ins, outs = refs[:4 * n], refs[4 * n:]
        for k in range(n):
            g, w, m, v = (r[...] for r in ins[4 * k:4 * k + 4])
            d, m1, v1 = _adam_math(g, w, m, v)
            outs[3 * k][...] = d
            outs[3 * k + 1][...] = m1
            outs[3 * k + 2][...] = v1

    flat = [a for it in items for a in it]
    shapes = [jax.ShapeDtypeStruct(it[1].shape, F32) for it in items for _ in range(3)]
    out = pl.pallas_call(
        body, name=name, out_shape=shapes,
        in_specs=[VMEM_SPEC] * (4 * n), out_specs=[VMEM_SPEC] * (3 * n),
        compiler_params=_cparams(),
    )(*flat)
    return [tuple(out[3 * k:3 * k + 3]) for k in range(n)]


def _adamw_tiled(g, w, m, v, name):
    rows, cols = w.shape
    tr = _row_tile(rows)

    def body(g_ref, w_ref, m_ref, v_ref, d_out, m_out, v_out):
        d, m1, v1 = _adam_math(g_ref[...], w_ref[...], m_ref[...], v_ref[...])
        d_out[...] = d
        m_out[...] = m1
        v_out[...] = v1

    spec = pl.BlockSpec((tr, cols), lambda i: (i, 0))
    return pl.pallas_call(
        body, name=name, grid=(rows // tr,),
        out_shape=[jax.ShapeDtypeStruct(w.shape, F32)] * 3,
        in_specs=[spec] * 4, out_specs=[spec] * 3,
        compiler_params=_cparams(dimension_semantics=("parallel",)),
    )(g, w, m, v)


def _mod_forward(cond, w_mod, b_mod_cols, name):
    def body(c_ref, w_ref, b_ref, o_ref):
        o_ref[...] = _dot(_silu(c_ref[...]), w_ref[...]) + b_ref[...]

    return pl.pallas_call(
        body, name=name, out_shape=jax.ShapeDtypeStruct((cond.shape[0], w_mod.shape[1]), F32),
        in_specs=[VMEM_SPEC] * 3, out_specs=VMEM_SPEC, compiler_params=_cparams(),
    )(cond, w_mod, b_mod_cols)


def _mod_backward(cond, w_mod, dmod_cols, name):
    def body(c_ref, w_ref, d_ref, gw_ref, gc_ref):
        s = _silu(c_ref[...])
        d = d_ref[...]
        gw_ref[...] = _dot_tn(s, d)
        gc_ref[...] = _dot_nt(d[8:16, :], w_ref[...])

    return pl.pallas_call(
        body, name=name,
        out_shape=[jax.ShapeDtypeStruct(w_mod.shape, F32), jax.ShapeDtypeStruct((8, w_mod.shape[0]), F32)],
        in_specs=[VMEM_SPEC] * 3, out_specs=[VMEM_SPEC] * 2, compiler_params=_cparams(),
    )(cond, w_mod, dmod_cols)


def _col_chunks(width, step=512):
    return [(s, min(step, width - s)) for s in range(0, width, step)]


def _in_projection(z, modc, modx, pre1, w_r, n_ctx_tiles, name):
    rows, d = z.shape
    width = w_r.shape[1]

    def body(z_ref, modc_ref, modx_ref, pre_ref, w_ref, h_ref, p_ref):
        is_ctx = pl.program_id(0) < n_ctx_tiles
        n, _ = _rms(z_ref[...])
        shift = jnp.where(is_ctx, modc_ref[0:1, :], modx_ref[0:1, :])
        scale = jnp.where(is_ctx, modc_ref[1:2, :], modx_ref[1:2, :])
        h = (n * pre_ref[...] * (1.0 + scale) + shift).astype(BF16)
        h_ref[...] = h
        for s, w in _col_chunks(width):
            p_ref[:, s:s + w] = jnp.dot(h, w_ref[:, s:s + w], preferred_element_type=F32)

    row = lambda i: (i, 0)
    fixed = lambda i: (0, 0)
    return pl.pallas_call(
        body, name=name, grid=(rows // TM,),
        out_shape=[jax.ShapeDtypeStruct((rows, d), BF16), jax.ShapeDtypeStruct((rows, width), F32)],
        in_specs=[pl.BlockSpec((TM, d), row), pl.BlockSpec((8, d), fixed), pl.BlockSpec((8, d), fixed),
                  pl.BlockSpec((1, d), fixed), VMEM_SPEC],
        out_specs=[pl.BlockSpec((TM, d), row), pl.BlockSpec((TM, width), row)],
        compiler_params=_cparams(dimension_semantics=("parallel",)),
    )(z, modc, modx, pre1, w_r)


C_HQ, C_HI, C_HF_FW, C_HF_BW, C_HGATE, C_GQ, C_GK, C_GV, C_GGATE = range(9)
OFF_GATE_HG = 9 * HW
OFF_LR = 13 * HW
P_WIDTH = OFF_LR + 128


def _head_norm_fwd(o, w):
    outs, ns, rs = [], [], []
    for h in range(NH):
        n, r = _rms(o[:, h * HD:(h + 1) * HD])
        ns.append(n)
        rs.append(r)
        outs.append(n * w)
    return jnp.concatenate(outs, axis=1), ns, rs


def _mixer_tail(z, o_hg, o_gla, p_hgate, p_ggate, p_gate_hg, p_gate_gla, hg_on, gla_on, wbh, wbg, wout):
    on_hg, n_hg, r_hg = _head_norm_fwd(o_hg, hg_on)
    on_gla, n_gla, r_gla = _head_norm_fwd(o_gla, gla_on)
    og_hg = (on_hg * _silu(p_hgate)).astype(BF16)
    og_gla = (on_gla * _silu(p_ggate)).astype(BF16)
    b_hg = jnp.dot(og_hg, wbh, preferred_element_type=F32)
    b_gla = jnp.dot(og_gla, wbg, preferred_element_type=F32)
    s_hg = _sigmoid(p_gate_hg)
    s_gla = _sigmoid(p_gate_gla)
    merged = (s_hg * b_hg + s_gla * b_gla).astype(BF16)
    y1 = jnp.dot(merged, wout, preferred_element_type=F32)
    return dict(on_hg=on_hg, n_hg=n_hg, r_hg=r_hg, on_gla=on_gla, n_gla=n_gla, r_gla=r_gla, og_hg=og_hg,
                og_gla=og_gla, b_hg=b_hg, b_gla=b_gla, s_hg=s_hg, s_gla=s_gla, merged=merged, y1=y1)


def _mixer_ffn(x_lat, p, o_list, modx, norms, onorms, w_br_hg, w_br_gla, w_out, w_gate, w_up, w_down, target,
               n_ctx_tiles, name):
    rows, d = x_lat.shape
    dff = w_gate.shape[1]
    inv_d = 1.0 / d

    def body(x_ref, ofw_hg, obw_hg, ofw_gla, obw_gla, p_hgate, p_ggate, p_ghg_a, p_ghg_b, p_ggla_a, p_ggla_b,
             modx_ref, norm_ref, on_ref, wbh_ref, wbg_ref, wout_ref, wg_ref, wu_ref, wd_ref, t_ref,
             loss_ref, dz2_ref, y1_ref, mrg_ref, oghg_ref, oggla_ref, h2_ref, a_ref, du_ref, dv_ref, dy2_ref,
             stat_ref):
        i = pl.program_id(0)
        post1, pre2, post2 = norm_ref[1:2, :], norm_ref[2:3, :], norm_ref[3:4, :]
        gate1, shift2, scale2, gate2 = modx_ref[2:3, :], modx_ref[3:4, :], modx_ref[4:5, :], modx_ref[5:6, :]
        p_gate_hg = jnp.concatenate([p_ghg_a[...], p_ghg_b[...]], axis=1)
        p_gate_gla = jnp.concatenate([p_ggla_a[...], p_ggla_b[...]], axis=1)
        t = _mixer_tail(x_ref[...], ofw_hg[...] + obw_hg[...], ofw_gla[...] + obw_gla[...], p_hgate[...],
                        p_ggate[...], p_gate_hg, p_gate_gla, on_ref[0:1, 0:HD], on_ref[1:2, 0:HD],
                        wbh_ref[...], wbg_ref[...], wout_ref[...])
        y1_ref[...] = t["y1"]
        mrg_ref[...] = t["merged"]
        oghg_ref[...] = t["og_hg"]
        oggla_ref[...] = t["og_gla"]
        n1, _ = _rms(t["y1"])
        z2 = x_ref[...] + n1 * post1 * gate1
        n2, r2 = _rms(z2)
        nw2 = n2 * pre2
        h2 = (nw2 * (1.0 + scale2) + shift2).astype(BF16)
        h2_ref[...] = h2
        u = jnp.dot(h2, wg_ref[...], preferred_element_type=F32)
        v = jnp.dot(h2, wu_ref[...], preferred_element_type=F32)
        su = _silu(u)
        a = (su * v).astype(BF16)
        a_ref[...] = a
        y2 = jnp.dot(a, wd_ref[...], preferred_element_type=F32)
        n3, r3 = _rms(y2)
        z3 = z2 + n3 * post2 * gate2
        err = z3 - t_ref[...]
        part = 0.5 * inv_d * jnp.sum(err * err)
        dz3 = err * inv_d
        dgate2 = _colsum(dz3 * n3 * post2)
        tt = dz3 * gate2
        dpost2 = _colsum(tt * n3)
        dy2 = _rms_bwd(tt * post2, n3, r3).astype(BF16)
        dy2_ref[...] = dy2
        da = _dot_nt(dy2, wd_ref[...])
        du = (da * v * _dsilu(u)).astype(BF16)
        dv = (da * su).astype(BF16)
        du_ref[...] = du
        dv_ref[...] = dv
        dh2 = _dot_nt(du, wg_ref[...]) + _dot_nt(dv, wu_ref[...])
        dshift2 = _colsum(dh2)
        dscale2 = _colsum(dh2 * nw2)
        dnw2 = dh2 * (1.0 + scale2)
        dpre2 = _colsum(dnw2 * n2)
        dz2_ref[...] = dz3 + _rms_bwd(dnw2 * pre2, n2, r2)

        @pl.when(i == 0)
        def _():
            stat_ref[...] = jnp.zeros_like(stat_ref)
            loss_ref[...] = jnp.zeros_like(loss_ref)

        for r, val in enumerate((dshift2, dscale2, dgate2, dpre2, dpost2)):
            stat_ref[r:r + 1, :] += val
        loss_ref[...] += part

    tm = TM_FFN
    ctx_tiles = n_ctx_tiles * (TM // tm)
    lat = lambda i: (i, 0)
    full = lambda i: (i + ctx_tiles, 0)
    fixed = lambda i: (0, 0)

    def pcol(blk):
        return pl.BlockSpec((tm, HW), lambda i: (i + ctx_tiles, blk))

    in_specs = ([pl.BlockSpec((tm, d), lat)] + [pl.BlockSpec((tm, HW), full)] * 4
                + [pcol(C_HGATE), pcol(C_GGATE), pcol(9), pcol(10), pcol(11), pcol(12)]
                + [pl.BlockSpec((8, d), fixed), pl.BlockSpec((8, d), fixed), pl.BlockSpec((8, d), fixed)]
                + [VMEM_SPEC] * 6 + [pl.BlockSpec((tm, d), lat)])
    bf = lambda w: jax.ShapeDtypeStruct((rows, w), BF16)
    out_shape = [jax.ShapeDtypeStruct((8, 128), F32), jax.ShapeDtypeStruct((rows, d), F32),
                 jax.ShapeDtypeStruct((rows, d), F32), bf(d), bf(HW), bf(HW), bf(d), bf(dff), bf(dff), bf(dff), bf(d),
                 jax.ShapeDtypeStruct((8, d), F32)]
    out_specs = [pl.BlockSpec((8, 128), fixed), pl.BlockSpec((tm, d), lat), pl.BlockSpec((tm, d), lat),
                 pl.BlockSpec((tm, d), lat), pl.BlockSpec((tm, HW), lat), pl.BlockSpec((tm, HW), lat),
                 pl.BlockSpec((tm, d), lat), pl.BlockSpec((tm, dff), lat), pl.BlockSpec((tm, dff), lat),
                 pl.BlockSpec((tm, dff), lat), pl.BlockSpec((tm, d), lat), pl.BlockSpec((8, d), fixed)]
    return pl.pallas_call(
        body, name=name, grid=(rows // tm,), out_shape=out_shape, in_specs=in_specs, out_specs=out_specs,
        compiler_params=_cparams(dimension_semantics=("arbitrary",)),
    )(x_lat, *o_list, p, p, p, p, p, p, modx, norms, onorms, w_br_hg, w_br_gla, w_out, w_gate, w_up, w_down, target)


def _mixer_tail_bwd(x_lat, p, o_list, dz2, y1, modx, norms, onorms, w_br_hg, w_br_gla, w_out, n_ctx_tiles, n_tiles,
                    name):
    rows, d = x_lat.shape
    total = n_tiles * TM

    def body(x_ref, ofw_hg, obw_hg, ofw_gla, obw_gla, p_hgate, p_ggate, p_ghg_a, p_ghg_b, p_ggla_a, p_ggla_b,
             dz2_ref, y1_ref, modx_ref, norm_ref, on_ref, wbh_ref, wbg_ref, wout_ref,
             dohg_ref, dogla_ref, dhgate_ref, dggate_ref, dghg_ref, dggla_ref, dy1_ref, dbhg_ref, dbgla_ref,
             stat_ref):
        i = pl.program_id(0)

        @pl.when(i == 0)
        def _():
            stat_ref[...] = jnp.zeros_like(stat_ref)

        @pl.when(i < n_ctx_tiles)
        def _():
            for ref in (dohg_ref, dogla_ref, dhgate_ref, dggate_ref, dghg_ref, dggla_ref):
                ref[...] = jnp.zeros_like(ref)

        @pl.when(i >= n_ctx_tiles)
        def _():
            post1, gate1 = norm_ref[1:2, :], modx_ref[2:3, :]
            hg_on, gla_on = on_ref[0:1, 0:HD], on_ref[1:2, 0:HD]
            p_gate_hg = jnp.concatenate([p_ghg_a[...], p_ghg_b[...]], axis=1)
            p_gate_gla = jnp.concatenate([p_ggla_a[...], p_ggla_b[...]], axis=1)
            ph, pg = p_hgate[...], p_ggate[...]
            t = _mixer_tail(x_ref[...], ofw_hg[...] + obw_hg[...], ofw_gla[...] + obw_gla[...], ph, pg,
                            p_gate_hg, p_gate_gla, hg_on, gla_on, wbh_ref[...], wbg_ref[...], wout_ref[...])
            dz2 = dz2_ref[...]
            n1, r1 = _rms(y1_ref[...])
            dgate1 = _colsum(dz2 * n1 * post1)
            tt = dz2 * gate1
            dpost1 = _colsum(tt * n1)
            dy1 = _rms_bwd(tt * post1, n1, r1).astype(BF16)
            dy1_ref[...] = dy1
            dmerged = _dot_nt(dy1, wout_ref[...])
            dghg_ref[...] = dmerged * t["b_hg"] * t["s_hg"] * (1.0 - t["s_hg"])
            dggla_ref[...] = dmerged * t["b_gla"] * t["s_gla"] * (1.0 - t["s_gla"])
            db_hg = (dmerged * t["s_hg"]).astype(BF16)
            db_gla = (dmerged * t["s_gla"]).astype(BF16)
            dbhg_ref[...] = db_hg
            dbgla_ref[...] = db_gla
            don_acc = []
            for (db, wb, pgate, on, ns, rs, gain, gate_ref, do_ref) in (
                    (db_hg, wbh_ref, ph, t["on_hg"], t["n_hg"], t["r_hg"], hg_on, dhgate_ref, dohg_ref),
                    (db_gla, wbg_ref, pg, t["on_gla"], t["n_gla"], t["r_gla"], gla_on, dggate_ref, dogla_ref)):
                dog = _dot_nt(db, wb[...])
                gate_ref[...] = dog * on * _dsilu(pgate)
                don = dog * _silu(pgate)
                acc = jnp.zeros((1, HD), F32)
                for h in range(NH):
                    sl = slice(h * HD, (h + 1) * HD)
                    acc = acc + _colsum(don[:, sl] * ns[h])
                    do_ref[:, sl] = _rms_bwd(don[:, sl] * gain, ns[h], rs[h])
                don_acc.append(acc)
            stat_ref[0:1, :] += dgate1
            stat_ref[1:2, :] += dpost1
            stat_ref[2:3, 0:HD] += don_acc[0]
            stat_ref[2:3, HD:2 * HD] += don_acc[1]

    lat = lambda i: (jnp.maximum(i - n_ctx_tiles, 0), 0)
    full = lambda i: (i, 0)
    fixed = lambda i: (0, 0)

    def pcol(blk):
        return pl.BlockSpec((TM, HW), lambda i: (i, blk))

    in_specs = ([pl.BlockSpec((TM, d), lat)] + [pl.BlockSpec((TM, HW), full)] * 4
                + [pcol(C_HGATE), pcol(C_GGATE), pcol(9), pcol(10), pcol(11), pcol(12)]
                + [pl.BlockSpec((TM, d), lat), pl.BlockSpec((TM, d), lat)]
                + [pl.BlockSpec((8, d), fixed)] * 3 + [VMEM_SPEC] * 3)
    f = lambda w: jax.ShapeDtypeStruct((total, w), F32)
    out_shape = [f(HW), f(HW), f(HW), f(HW), f(d), f(d), jax.ShapeDtypeStruct((rows, d), BF16),
                 jax.ShapeDtypeStruct((rows, d), BF16), jax.ShapeDtypeStruct((rows, d), BF16),
                 jax.ShapeDtypeStruct((8, d), F32)]
    out_specs = ([pl.BlockSpec((TM, HW), full)] * 4 + [pl.BlockSpec((TM, d), full)] * 2
                 + [pl.BlockSpec((TM, d), lat)] * 3 + [pl.BlockSpec((8, d), fixed)])
    return pl.pallas_call(
        body, name=name, grid=(n_tiles,), out_shape=out_shape, in_specs=in_specs, out_specs=out_specs,
        compiler_params=_cparams(dimension_semantics=("arbitrary",)),
    )(x_lat, *o_list, p, p, p, p, p, p, dz2, y1, modx, norms, onorms, w_br_hg, w_br_gla, w_out)


def _in_projection_bwd(z, dz2, modc, modx, pre1, w_r, pieces, n_ctx_tiles, name):
    rows, d = z.shape
    lat_rows = dz2.shape[0]
    width = w_r.shape[1]
    n_pieces = len(pieces)

    def body(*refs):
        z_ref, dz2_ref, modc_ref, modx_ref, pre_ref, w_ref = refs[:6]
        (dhq_f, dhq_b, dhi_f, dhi_b, dhf_f, dhf_b, dhgate, dgq_f, dgq_b, dgk_f, dgk_b, dgv_f, dgv_b, dggate,
         dghg, dggla, dlr_f, dlr_b) = refs[6:6 + n_pieces]
        dp_ref, gx_ref, stat_ref = refs[6 + n_pieces:]
        i = pl.program_id(0)
        is_ctx = i < n_ctx_tiles
        sections = [
            (0, dhq_f[...] + dhq_b[...]), (HW, dhi_f[...] + dhi_b[...]), (2 * HW, dhf_f[...]), (3 * HW, dhf_b[...]),
            (4 * HW, dhgate[...]), (5 * HW, dgq_f[...] + dgq_b[...]), (6 * HW, dgk_f[...] + dgk_b[...]),
            (7 * HW, dgv_f[...] + dgv_b[...]), (8 * HW, dggate[...]),
            (9 * HW, dghg[:, 0:HW]), (10 * HW, dghg[:, HW:2 * HW]),
            (11 * HW, dggla[:, 0:HW]), (12 * HW, dggla[:, HW:2 * HW]), (OFF_LR, dlr_f[...] + dlr_b[...])]
        dh = jnp.zeros((TM, d), F32)
        for off, val in sections:
            w = val.shape[1]
            vb = val.astype(BF16)
            dp_ref[:, off:off + w] = vb
            dh = dh + _dot_nt(vb, w_ref[:, off:off + w])
        n, r = _rms(z_ref[...])
        pre = pre_ref[...]
        scale = jnp.where(is_ctx, modc_ref[1:2, :], modx_ref[1:2, :])
        nw = n * pre
        dshift = _colsum(dh)
        dscale = _colsum(dh * nw)
        dnw = dh * (1.0 + scale)
        dpre = _colsum(dnw * n)
        gx_ref[...] = dz2_ref[...] + _rms_bwd(dnw * pre, n, r)
        zero = jnp.zeros((1, d), F32)

        @pl.when(i == 0)
        def _():
            stat_ref[...] = jnp.zeros_like(stat_ref)

        stat_ref[0:1, :] += jnp.where(is_ctx, zero, dshift)
        stat_ref[1:2, :] += jnp.where(is_ctx, zero, dscale)
        stat_ref[2:3, :] += jnp.where(is_ctx, dshift, zero)
        stat_ref[3:4, :] += jnp.where(is_ctx, dscale, zero)
        stat_ref[4:5, :] += dpre

    full = lambda i: (i, 0)
    lat = lambda i: (jnp.maximum(i - n_ctx_tiles, 0), 0)
    fixed = lambda i: (0, 0)
    piece_specs = [pl.BlockSpec((TM, a.shape[1]), full) for a in pieces]
    in_specs = [pl.BlockSpec((TM, d), full), pl.BlockSpec((TM, d), lat), pl.BlockSpec((8, d), fixed),
                pl.BlockSpec((8, d), fixed), pl.BlockSpec((1, d), fixed), VMEM_SPEC] + piece_specs
    return pl.pallas_call(
        body, name=name, grid=(rows // TM,),
        out_shape=[jax.ShapeDtypeStruct((rows, width), BF16), jax.ShapeDtypeStruct((lat_rows, d), F32),
                   jax.ShapeDtypeStruct((8, d), F32)],
        in_specs=in_specs,
        out_specs=[pl.BlockSpec((TM, width), full), pl.BlockSpec((TM, d), lat), pl.BlockSpec((8, d), fixed)],
        compiler_params=_cparams(dimension_semantics=("arbitrary",)),
    )(z, dz2, modc, modx, pre1, w_r, *pieces)


def _weight_grad(xs, dy, name, col_block=None, tk=256, tn=512):
    rows = dy.shape[0]
    k = xs.shape[1]
    if col_block is None:
        n, tn_, cb = dy.shape[1], tn, 0
    else:
        n, tn_, cb = col_block[0], col_block[0], col_block[1]
    tn_ = min(tn_, n)
    tk_ = min(tk, k)

    def body(x_ref, dy_ref, o_ref):
        o_ref[...] = _dot_tn(x_ref[...], dy_ref[...])

    return pl.pallas_call(
        body, name=name, grid=(k // tk_, n // tn_),
        out_shape=jax.ShapeDtypeStruct((k, n), F32),
        in_specs=[pl.BlockSpec((rows, tk_), lambda i, j: (0, i)),
                  pl.BlockSpec((rows, tn_), lambda i, j: (0, j + cb))],
        out_specs=pl.BlockSpec((tk_, tn_), lambda i, j: (i, j)),
        compiler_params=_cparams(dimension_semantics=("parallel", "parallel")),
    )(xs, dy)


def _chunk_terms(q, k, g, fw):
    c = CHUNK
    r = lax.broadcasted_iota(jnp.int32, (c, c), 0)
    s = lax.broadcasted_iota(jnp.int32, (c, c), 1)
    causal = (s <= r) if fw else (s >= r)
    causal_t = (s >= r) if fw else (s <= r)
    cum = jnp.dot(causal.astype(F32), g, precision=HIGHEST, preferred_element_type=F32)
    row = lax.broadcasted_iota(jnp.int32, (c, 1), 0)
    pos = row if fw else (c - 1 - row)
    starts = [None]
    for j in range(1, NSUB):
        rj = SUB * j - 1 if fw else c - SUB * j
        starts.append(cum[rj:rj + 1, :])
    in_blk = [(pos >= SUB * j) & (pos < SUB * (j + 1)) for j in range(NSUB)]
    e = [jnp.exp(cum)]
    for j in range(1, NSUB):
        e.append(jnp.exp(jnp.where(pos >= SUB * j, cum - starts[j], -1e30)))
    own = jnp.zeros_like(cum)
    for j in range(1, NSUB):
        own = own + jnp.where(in_blk[j], starts[j], 0.0)
    kscale = jnp.exp(own - cum)
    rend = c - 1 if fw else 0
    cend = cum[rend:rend + 1, :]
    tail = jnp.exp(cend - cum)
    qcat = jnp.concatenate([q * e[j] for j in range(NSUB)], axis=1).astype(BF16)
    kt = k * kscale
    km = jnp.concatenate([jnp.where(in_blk[j], kt, 0.0) for j in range(NSUB)], axis=1).astype(BF16)
    return dict(causal=causal, causal_t=causal_t, e=e, in_blk=in_blk, kscale=kscale, cend=cend, tail=tail,
                qcat=qcat, km=km)


def _chunk_fwd(q, k, v, g, st0, fw):
    t = _chunk_terms(q, k, g, fw)
    a = jnp.where(t["causal"], _dot_nt(t["qcat"], t["km"]), 0.0)
    o = _dot(a, v) + _dot_nt(t["qcat"][:, 0:HD], st0)
    st1 = st0 * jnp.exp(t["cend"]) + _dot_tn(v, k * t["tail"])
    return o, st1


def _chunk_bwd(q, k, v, g, st0, do, dst1, fw):
    t = _chunk_terms(q, k, g, fw)
    qcat, km, e = t["qcat"], t["km"], t["e"]
    a_t = jnp.where(t["causal_t"], _dot_nt(km, qcat), 0.0)
    ktail = k * t["tail"]
    dv = _dot(a_t, do) + _dot_nt(ktail, dst1)
    da = jnp.where(t["causal"], _dot_nt(do, v), 0.0)
    da_t = jnp.where(t["causal_t"], _dot_nt(v, do), 0.0)
    dqcat = _dot(da, km)
    dq_inter = e[0] * _dot(do, st0)
    dq = dq_inter
    for j in range(NSUB):
        dq = dq + e[j] * dqcat[:, j * HD:(j + 1) * HD]
    dkm = _dot(da_t, qcat)
    dkt = jnp.zeros_like(k)
    for j in range(NSUB):
        dkt = dkt + jnp.where(t["in_blk"][j], dkm[:, j * HD:(j + 1) * HD], 0.0)
    dk_inter = _dot(v, dst1) * t["tail"]
    dk = dkt * t["kscale"] + dk_inter
    dcum = q * dq_inter - k * dk_inter
    for j in range(NSUB):
        sl = slice(j * HD, (j + 1) * HD)
        dcum = dcum + qcat[:, sl].astype(F32) * dqcat[:, sl] - km[:, sl].astype(F32) * dkm[:, sl]
    ecend = jnp.exp(t["cend"])
    end = ecend * _colsum(st0 * dst1) + _colsum(k * dk_inter)
    dg = jnp.dot(t["causal_t"].astype(F32), dcum, precision=HIGHEST, preferred_element_type=F32) + end
    dst0 = dst1 * ecend + _dot_tn(do, q * e[0])
    return dq, dk, dv, dg, dst0


def _chunk_index(step, n_ctx_chunks, n_chunks, fw):
    if fw:
        return step
    return jnp.where(step < n_ctx_chunks, n_ctx_chunks - 1 - step, n_chunks - 1 + n_ctx_chunks - step)


def _hg_inputs(hq, hf, lbv, d_idx, sl):
    lb = _sigmoid(lbv[d_idx:d_idx + 1, sl] - lbv[2 + d_idx:3 + d_idx, sl])
    sg = _sigmoid(hf)
    f = lb + (1.0 - lb) * sg
    return _silu(hq), 1.0 - f, jnp.log(f), f, sg, lb


def _scan_fwd(p, side, n_ctx_chunks, fw, branch, name):
    rows = p.shape[0]
    n_chunks = rows // CHUNK
    d_idx = 0 if fw else 1
    hg = branch == "hg"
    cols = (C_HQ, C_HI, C_HF_FW + d_idx) if hg else (C_GQ, C_GK, C_GV)

    def body(*refs):
        if hg:
            a_ref, b_ref, c_ref, lb_ref, o_ref, st_ref, state = refs
        else:
            a_ref, b_ref, c_ref, lr_ref, wgk_ref, bgk_ref, o_ref, st_ref, state = refs
            logits = _dot(lr_ref[...], wgk_ref[...]) + bgk_ref[...]
            g_all = _log_sigmoid(logits) * (1.0 / GATE_NORM)

        @pl.when(pl.program_id(0) == 0)
        def _():
            state[...] = jnp.zeros_like(state)

        for h in range(NH):
            sl = slice(h * HD, (h + 1) * HD)
            if hg:
                q, k, g, _, _, _ = _hg_inputs(a_ref[:, sl], c_ref[:, sl], lb_ref[...], d_idx, sl)
                v = b_ref[:, sl]
            else:
                q, k, v, g = a_ref[:, sl] * (HD ** -0.5), b_ref[:, sl], c_ref[:, sl], g_all[:, sl]
            st0 = state[h]
            st_ref[0, h] = st0
            o, st1 = _chunk_fwd(q, k, v, g, st0, fw)
            o_ref[:, sl] = o
            state[h] = st1

    def cmap(blk):
        return pl.BlockSpec((CHUNK, HW), lambda j: (_chunk_index(j, n_ctx_chunks, n_chunks, fw), blk))

    fixed = lambda j: (0, 0)
    in_specs = [cmap(cols[0]), cmap(cols[1]), cmap(cols[2])]
    if hg:
        in_specs += [pl.BlockSpec((4, HW), fixed)]
        args = (p, p, p, side)
    else:
        in_specs += [pl.BlockSpec((CHUNK, 128), lambda j: (_chunk_index(j, n_ctx_chunks, n_chunks, fw), OFF_LR // 128)),
                     pl.BlockSpec((128, HW), fixed), pl.BlockSpec((1, HW), fixed)]
        args = (p, p, p, p, side[0], side[1])
    return pl.pallas_call(
        body, name=name, grid=(n_chunks,),
        out_shape=[jax.ShapeDtypeStruct((rows, HW), F32), jax.ShapeDtypeStruct((n_chunks, NH, HD, HD), F32)],
        in_specs=in_specs,
        out_specs=[pl.BlockSpec((CHUNK, HW), lambda j: (_chunk_index(j, n_ctx_chunks, n_chunks, fw), 0)),
                   pl.BlockSpec((1, NH, HD, HD), lambda j: (_chunk_index(j, n_ctx_chunks, n_chunks, fw), 0, 0, 0))],
        scratch_shapes=[pltpu.VMEM((NH, HD, HD), F32)],
        compiler_params=_cparams(dimension_semantics=("arbitrary",)),
    )(*args)


def _scan_bwd(p, side, states, d_o, n_ctx_chunks, fw, branch, name):
    rows = p.shape[0]
    n_chunks = rows // CHUNK
    d_idx = 0 if fw else 1
    hg = branch == "hg"
    cols = (C_HQ, C_HI, C_HF_FW + d_idx) if hg else (C_GQ, C_GK, C_GV)

    def body(*refs):
        if hg:
            a_ref, b_ref, c_ref, lb_ref, st_ref, do_ref, da_ref, db_ref, dc_ref, dlb_ref, dstate = refs
        else:
            (a_ref, b_ref, c_ref, lr_ref, wgk_ref, bgk_ref, st_ref, do_ref, da_ref, db_ref, dc_ref, dlr_ref,
             dwgk_ref, dbias_ref, dstate) = refs
            lr = lr_ref[...]
            logits = _dot(lr, wgk_ref[...]) + bgk_ref[...]
            g_all = _log_sigmoid(logits) * (1.0 / GATE_NORM)

        @pl.when(pl.program_id(0) == 0)
        def _():
            dstate[...] = jnp.zeros_like(dstate)
            if hg:
                dlb_ref[...] = jnp.zeros_like(dlb_ref)
            else:
                dwgk_ref[...] = jnp.zeros_like(dwgk_ref)
                dbias_ref[...] = jnp.zeros_like(dbias_ref)

        dg_parts = []
        for h in range(NH):
            sl = slice(h * HD, (h + 1) * HD)
            if hg:
                hq, hf = a_ref[:, sl], c_ref[:, sl]
                q, k, g, f, sg, lb = _hg_inputs(hq, hf, lb_ref[...], d_idx, sl)
                v = b_ref[:, sl]
            else:
                q, k, v, g = a_ref[:, sl] * (HD ** -0.5), b_ref[:, sl], c_ref[:, sl], g_all[:, sl]
            dq, dk, dv, dg, dst0 = _chunk_bwd(q, k, v, g, st_ref[0, h], do_ref[:, sl], dstate[h], fw)
            dstate[h] = dst0
            if hg:
                da_ref[:, sl] = dq * _dsilu(hq)
                db_ref[:, sl] = dv
                df = dg / f - dk
                dc_ref[:, sl] = df * (1.0 - lb) * sg * (1.0 - sg)
                dlb_ref[0:1, sl] += _colsum(df * (1.0 - sg))
            else:
                da_ref[:, sl] = dq * (HD ** -0.5)
                db_ref[:, sl] = dk
                dc_ref[:, sl] = dv
                dg_parts.append(dg)
        if not hg:
            dlogits = jnp.concatenate(dg_parts, axis=1) * (1.0 / GATE_NORM) * (1.0 - _sigmoid(logits))
            dlr_ref[...] = _dot_nt(dlogits, wgk_ref[...])
            dwgk_ref[...] += _dot_tn(lr, dlogits)
            dbias_ref[0:1, :] += _colsum(dlogits)

    def chunk_of(j):
        return _chunk_index(n_chunks - 1 - j, n_ctx_chunks, n_chunks, fw)

    def cmap(blk, width=HW):
        return pl.BlockSpec((CHUNK, width), lambda j: (chunk_of(j), blk))

    fixed = lambda j: (0, 0)
    st_spec = pl.BlockSpec((1, NH, HD, HD), lambda j: (chunk_of(j), 0, 0, 0))
    big = jax.ShapeDtypeStruct((rows, HW), F32)
    if hg:
        in_specs = [cmap(cols[0]), cmap(cols[1]), cmap(cols[2]), pl.BlockSpec((4, HW), fixed), st_spec, cmap(0)]
        args = (p, p, p, side, states, d_o)
        out_shape = [big, big, big, jax.ShapeDtypeStruct((8, HW), F32)]
        out_specs = [cmap(0), cmap(0), cmap(0), pl.BlockSpec((8, HW), fixed)]
    else:
        in_specs = [cmap(cols[0]), cmap(cols[1]), cmap(cols[2]), cmap(OFF_LR // 128, 128),
                    pl.BlockSpec((128, HW), fixed), pl.BlockSpec((1, HW), fixed), st_spec, cmap(0)]
        args = (p, p, p, p, side[0], side[1], states, d_o)
        out_shape = [big, big, big, jax.ShapeDtypeStruct((rows, 128), F32), jax.ShapeDtypeStruct((128, HW), F32),
                     jax.ShapeDtypeStruct((8, HW), F32)]
        out_specs = [cmap(0), cmap(0), cmap(0), cmap(0, 128), pl.BlockSpec((128, HW), fixed),
                     pl.BlockSpec((8, HW), fixed)]
    return pl.pallas_call(
        body, name=name, grid=(n_chunks,), out_shape=out_shape, in_specs=in_specs, out_specs=out_specs,
        scratch_shapes=[pltpu.VMEM((NH, HD, HD), F32)],
        compiler_params=_cparams(dimension_semantics=("arbitrary",)),
    )(*args)


SMALL_ROWS = 56
ROWS_MOD_X = (0, 1, 8, 16, 17, 18)
ROWS_MOD_C = (2, 3)
ROW_PRE1, ROW_POST1, ROW_ONORM, ROW_PRE2, ROW_POST2, ROW_LB, ROW_BGK, ROW_WGK = 4, 9, 10, 19, 20, 24, 32, 40


def _reduce_small(gathered, lb_full, name):
    _, _, d = gathered.shape

    def body(g_ref, lb_ref, sum_ref, dmod_ref, dbmod_ref, dlb_ref):
        total = g_ref[0]
        for b in range(1, N_DEV):
            total = total + g_ref[b]
        sum_ref[...] = total
        dmod_ref[...] = jnp.zeros_like(dmod_ref)
        for m in range(N_MOD):
            col = slice(m * d, (m + 1) * d)
            acc = jnp.zeros((1, d), F32)
            for b in range(N_DEV):
                row = g_ref[b, ROWS_MOD_X[m]:ROWS_MOD_X[m] + 1, :]
                dmod_ref[b:b + 1, col] = row
                acc = acc + row
            if m < 2:
                ctx_row = total[ROWS_MOD_C[m]:ROWS_MOD_C[m] + 1, :]
                dmod_ref[8:9, col] = ctx_row
                acc = acc + ctx_row
            dbmod_ref[:, col] = acc
        lbv = lb_ref[...]
        for dd in range(2):
            lb = _sigmoid(lbv[dd:dd + 1, :] - lbv[2 + dd:3 + dd, :])
            gl = total[ROW_LB:ROW_LB + 1, dd * HW:(dd + 1) * HW] * lb * (1.0 - lb)
            dlb_ref[dd:dd + 1, :] = gl
            dlb_ref[2 + dd:3 + dd, :] = -gl

    return pl.pallas_call(
        body, name=name,
        out_shape=[jax.ShapeDtypeStruct((SMALL_ROWS, d), F32), jax.ShapeDtypeStruct((16, N_MOD * d), F32),
                   jax.ShapeDtypeStruct((1, N_MOD * d), F32), jax.ShapeDtypeStruct((4, HW), F32)],
        in_specs=[VMEM_SPEC] * 2, out_specs=[VMEM_SPEC] * 4, compiler_params=_cparams(),
    )(gathered, lb_full)


def _c_ctx_grad(gathered, c_ctx_row, name):
    def body(g_ref, c_ref, o_ref):
        acc = g_ref[0, 0:1, :]
        for chip in range(1, N_CHIP):
            acc = acc + g_ref[2 * chip, 0:1, :]
        o_ref[...] = acc * _dsilu(c_ref[...])

    return pl.pallas_call(
        body, name=name, out_shape=jax.ShapeDtypeStruct(c_ctx_row.shape, F32),
        in_specs=[VMEM_SPEC] * 2, out_specs=VMEM_SPEC, compiler_params=_cparams(),
    )(gathered, c_ctx_row)


def _relayout_w_in(w):
    pad = jnp.zeros((w.shape[0], 128 - 2 * RANK), w.dtype)
    return jnp.concatenate([w[:, :9 * HW], w[:, 9 * HW + 2 * RANK:], w[:, 9 * HW:9 * HW + 2 * RANK], pad], axis=1)


def _unrelayout_w_in(g_main, g_lr):
    return jnp.concatenate([g_main[:, :9 * HW], g_lr[:, :2 * RANK], g_main[:, 9 * HW:]], axis=1)


def _blocked(full, n_blocks):
    k, n = full.shape
    return full.reshape(k, n_blocks, n // n_blocks).transpose(1, 0, 2)


def _unblocked(blocks):
    nb, k, n = blocks.shape
    return blocks.transpose(1, 0, 2).reshape(k, nb * n)


def _sample_step(x0, ctx0, target0, modc, modx, norm_pre1, norms, onorms, lb_full, gla_side, w_in_r, wbh, wbg, wout,
                 wg, wu, wd):
    seq, d = x0.shape
    ctx_len = ctx0.shape[0]
    n_ctx_tiles = ctx_len // TM
    n_tiles = (ctx_len + seq) // TM
    n_ctx_chunks = ctx_len // CHUNK
    z = jnp.concatenate([ctx0, x0], axis=0)
    h1, p = _in_projection(z, modc, modx, norm_pre1, w_in_r, n_ctx_tiles, "in_projection")
    o_hg_fw, st_hg_fw = _scan_fwd(p, lb_full, n_ctx_chunks, True, "hg", "scan_hg_fw")
    o_hg_bw, st_hg_bw = _scan_fwd(p, lb_full, n_ctx_chunks, False, "hg", "scan_hg_bw")
    o_gla_fw, st_gla_fw = _scan_fwd(p, gla_side[0], n_ctx_chunks, True, "gla", "scan_gla_fw")
    o_gla_bw, st_gla_bw = _scan_fwd(p, gla_side[1], n_ctx_chunks, False, "gla", "scan_gla_bw")
    o_list = [o_hg_fw, o_hg_bw, o_gla_fw, o_gla_bw]
    (loss_part, dz2, y1, merged, og_hg, og_gla, h2, a_act, du, dv, dy2, stat_ffn) = _mixer_ffn(
        x0, p, o_list, modx, norms, onorms, wbh, wbg, wout, wg, wu, wd, target0, n_ctx_tiles, "mixer_ffn")

    (d_ohg, d_ogla, d_hgate, d_ggate, d_ghg, d_ggla, dy1, db_hg, db_gla, stat_mix) = _mixer_tail_bwd(
        x0, p, o_list, dz2, y1, modx, norms, onorms, wbh, wbg, wout, n_ctx_tiles, n_tiles, "mixer_tail_bwd")
    dhq_f, dhi_f, dhf_f, dlb_f = _scan_bwd(p, lb_full, st_hg_fw, d_ohg, n_ctx_chunks, True, "hg", "scan_hg_fw_bwd")
    dhq_b, dhi_b, dhf_b, dlb_b = _scan_bwd(p, lb_full, st_hg_bw, d_ohg, n_ctx_chunks, False, "hg", "scan_hg_bw_bwd")
    dgq_f, dgk_f, dgv_f, dlr_f, dwgk_f, dbgk_f = _scan_bwd(p, gla_side[0], st_gla_fw, d_ogla, n_ctx_chunks, True, "gla",
                                                           "scan_gla_fw_bwd")
    dgq_b, dgk_b, dgv_b, dlr_b, dwgk_b, dbgk_b = _scan_bwd(p, gla_side[1], st_gla_bw, d_ogla, n_ctx_chunks, False, "gla",
                                                           "scan_gla_bw_bwd")
    pieces = [dhq_f, dhq_b, dhi_f, dhi_b, dhf_f, dhf_b, d_hgate, dgq_f, dgq_b, dgk_f, dgk_b, dgv_f, dgv_b, d_ggate,
              d_ghg, d_ggla, dlr_f, dlr_b]
    dp, grad_x, stat_in = _in_projection_bwd(z, dz2, modc, modx, norm_pre1, w_in_r, pieces, n_ctx_tiles,
                                             "in_projection_bwd")

    g_in_main = _weight_grad_cols(h1, dp, OFF_LR, "grad_w_in_main")
    g_in_lr = _weight_grad(h1, dp, "grad_w_in_lr", col_block=(128, OFF_LR // 128))
    dff = wg.shape[1]
    tn_ff = dff // N_CHIP if (dff // N_CHIP) % 128 == 0 else 256
    return dict(
        loss_part=loss_part, grad_x=grad_x, stat_in=stat_in, stat_mix=stat_mix, stat_ffn=stat_ffn,
        dlb=(dlb_f, dlb_b), dwgk=(dwgk_f, dwgk_b), dbgk=(dbgk_f, dbgk_b),
        g_w_in=_unrelayout_w_in(g_in_main, g_in_lr),
        g_br_hg=_weight_grad(og_hg, db_hg, "grad_w_br_hg"),
        g_br_gla=_weight_grad(og_gla, db_gla, "grad_w_br_gla"),
        g_out=_weight_grad(merged, dy1, "grad_w_out"),
        g_gate=_weight_grad(h2, du, "grad_w_ff_gate", tn=tn_ff),
        g_up=_weight_grad(h2, dv, "grad_w_ff_up", tn=tn_ff),
        g_down=_weight_grad(a_act, dy2, "grad_w_ff_down", tk=tn_ff))


def kernel(x, c, ctx, c_ctx, w_mod, b_mod, norm_pre1, norm_post1, norm_pre2, norm_post2, w_in, hg_lb, hg_onorm, gla_w_gk, gla_b_gk, gla_onorm, w_br_hg, w_br_gla, w_out, w_ff_gate, w_ff_up, w_ff_down, loss_target, m_c_ctx, m_w_mod, m_b_mod, m_norm_pre1, m_norm_post1, m_norm_pre2, m_norm_post2, m_w_in, m_hg_lb, m_hg_onorm, m_gla_w_gk, m_gla_b_gk, m_gla_onorm, m_w_br_hg, m_w_br_gla, m_w_out, m_w_ff_gate, m_w_ff_up, m_w_ff_down, v_c_ctx, v_w_mod, v_b_mod, v_norm_pre1, v_norm_post1, v_norm_pre2, v_norm_post2, v_w_in, v_hg_lb, v_hg_onorm, v_gla_w_gk, v_gla_b_gk, v_gla_onorm, v_w_br_hg, v_w_br_gla, v_w_out, v_w_ff_gate, v_w_ff_up, v_w_ff_down):
    seq, d = x.shape[1], x.shape[2]
    ctx_len = ctx.shape[1]
    assert seq % TM == 0 and ctx_len % TM == 0 and d == 2 * HW
    ax, ay, ac = lax.axis_index("x"), lax.axis_index("y"), lax.axis_index("c")
    chip = 2 * ax + ay
    dev = 2 * chip + ac
    c_arr = jnp.reshape(ac, (1,)).astype(jnp.int32)

    nc = d // 128
    pad8 = lambda a: jnp.pad(a, ((0, -a.shape[0] % 8), (0, 0)))
    small1 = jnp.concatenate([c.reshape(nc, 128), pad8(hg_lb.reshape(4, 128)), gla_w_gk.reshape(2 * RANK, 128),
                              pad8(gla_b_gk.reshape(2, 128))], axis=0)
    got1 = _allgather8(small1, "gather_small_params")
    c_all = got1[:, :nc, :].reshape(N_DEV, d)
    per_chip = got1[0::2]
    lb_full = per_chip[:, nc:nc + 4, :].transpose(1, 0, 2).reshape(4, HW)
    wgk_full = per_chip[:, nc + 8:nc + 8 + 2 * RANK, :].transpose(1, 0, 2).reshape(2, RANK, HW)
    bgk_full = per_chip[:, nc + 8 + 2 * RANK:nc + 10 + 2 * RANK, :].transpose(1, 0, 2).reshape(2, HW)
    wgk_pad = [jnp.zeros((128, HW), F32).at[dd * RANK:(dd + 1) * RANK].set(wgk_full[dd]) for dd in range(2)]
    bgk = [bgk_full[dd:dd + 1] for dd in range(2)]

    n_mod_cols = w_mod.shape[2]
    cond = jnp.concatenate([c_all, pad8(c_ctx.reshape(1, d))], axis=0)
    b_cols = lax.dynamic_slice(b_mod, (0, chip * n_mod_cols), (1, n_mod_cols))
    mod_part = _mod_forward(cond, w_mod[0], b_cols, "mod_forward")
    mod_got = _allgather8(mod_part, "gather_mod")
    mod_all = mod_got[0::2].transpose(1, 0, 2).reshape(16, N_CHIP * n_mod_cols)
    modx = pad8(lax.dynamic_slice(mod_all, (dev, 0), (1, N_MOD * d)).reshape(N_MOD, d))
    modc = pad8(mod_all[8].reshape(N_MOD, d))

    shards = [_cast_bf16(w_in[0], "cast_w_in"), _cast_bf16(w_br_hg[0], "cast_w_br_hg"),
              _cast_bf16(w_br_gla[0], "cast_w_br_gla"), _cast_bf16(w_out[0], "cast_w_out"),
              _cast_bf16(w_ff_gate[0], "cast_w_ff_gate"), _cast_bf16(w_ff_up[0], "cast_w_ff_up"),
              _cast_bf16(w_ff_down[0], "cast_w_ff_down")]
    gathered = _gather_shards(shards, "gather_weights")
    w_in_r = _relayout_w_in(_unblocked(gathered[0]))
    wbh, wbg = _unblocked(gathered[1]), _unblocked(gathered[2])
    wout = gathered[3].reshape(d, d)
    wg, wu = _unblocked(gathered[4]), _unblocked(gathered[5])
    wd = gathered[6].reshape(wg.shape[1], d)

    norms = jnp.concatenate([norm_pre1, norm_post1, norm_pre2, norm_post2, jnp.zeros((4, d), F32)], axis=0)
    onorms = jnp.zeros((8, d), F32).at[0, :HD].set(hg_onorm[0]).at[1, :HD].set(gla_onorm[0])
    gla_side = [(wgk_pad[dd], bgk[dd]) for dd in range(2)]
    r = _sample_step(x[0], ctx[0], loss_target[0], modc, modx, norm_pre1, norms, onorms, lb_full, gla_side, w_in_r,
                     wbh, wbg, wout, wg, wu, wd)
    loss_part, grad_x, stat_in, stat_mix, stat_ffn = (r[k] for k in ("loss_part", "grad_x", "stat_in", "stat_mix",
                                                                     "stat_ffn"))
    (dlb_f, dlb_b), (dwgk_f, dwgk_b), (dbgk_f, dbgk_b) = r["dlb"], r["dwgk"], r["dbgk"]
    dff = wg.shape[1]

    full = [_blocked(r["g_w_in"], N_CHIP), _blocked(r["g_br_hg"], N_CHIP), _blocked(r["g_br_gla"], N_CHIP),
            r["g_out"].reshape(N_CHIP, d // N_CHIP, d), _blocked(r["g_gate"], N_CHIP), _blocked(r["g_up"], N_CHIP),
            r["g_down"].reshape(N_CHIP, dff // N_CHIP, d)]
    from_sibling = _send_other_half(full, "grads_to_sibling")
    names = ["w_in", "w_br_hg", "w_br_gla", "w_out", "w_ff_gate", "w_ff_up", "w_ff_down"]
    pairs = [_pair_sum(c_arr, f, r, "pair_sum_" + nm) for f, r, nm in zip(full, from_sibling, names)]
    at_owner = _blocks_to_owner(pairs, "grads_to_owner")
    own_half = [_sum_chips(g, "chip_sum_" + nm) for g, nm in zip(at_owner, names)]
    other_half = _swap_with_sibling(own_half, "halves_to_sibling")
    big_w = [w_in, w_br_hg, w_br_gla, w_out, w_ff_gate, w_ff_up, w_ff_down]
    big_m = [m_w_in, m_w_br_hg, m_w_br_gla, m_w_out, m_w_ff_gate, m_w_ff_up, m_w_ff_down]
    big_v = [v_w_in, v_w_br_hg, v_w_br_gla, v_w_out, v_w_ff_gate, v_w_ff_up, v_w_ff_down]
    big = {}
    for nm, own, oth, w_, m_, v_ in zip(names, own_half, other_half, big_w, big_m, big_v):
        res = _adamw_halves(c_arr, own, oth, w_[0], m_[0], v_[0], "adamw_" + nm)
        big[nm] = [r[None] for r in res]

    small2 = jnp.concatenate([
        stat_in, stat_mix, stat_ffn, jnp.concatenate([dlb_f, dlb_b], axis=1), jnp.concatenate([dbgk_f, dbgk_b], axis=1),
        jnp.concatenate([dwgk_f[0:RANK], dwgk_b[RANK:2 * RANK]], axis=1)], axis=0)
    assert small2.shape[0] == SMALL_ROWS
    got2 = _allgather8(small2, "gather_small_grads")
    total, dmod_all, g_b_mod, g_lb_full = _reduce_small(got2, lb_full, "reduce_small")
    dmod_cols = lax.dynamic_slice(dmod_all, (0, chip * n_mod_cols), (16, n_mod_cols))
    g_w_mod, cctx_part = _mod_backward(cond, w_mod[0], dmod_cols, "mod_backward")
    got3 = _allgather8(cctx_part, "gather_c_ctx_grad")
    g_c_ctx = _c_ctx_grad(got3, c_ctx.reshape(1, d), "c_ctx_grad")

    g_pre1, g_post1, g_pre2, g_post2 = (total[r_:r_ + 1] for r_ in (ROW_PRE1, ROW_POST1, ROW_PRE2, ROW_POST2))
    g_hg_on, g_gla_on = total[ROW_ONORM:ROW_ONORM + 1, 0:HD], total[ROW_ONORM:ROW_ONORM + 1, HD:2 * HD]
    n_lb = hg_lb.shape[2]
    g_hg_lb = lax.dynamic_slice(g_lb_full, (0, chip * n_lb), (4, n_lb))
    g_bgk = lax.dynamic_slice(total[ROW_BGK:ROW_BGK + 1].reshape(2, HW), (0, chip * n_lb), (2, n_lb))
    g_wgk_full = total[ROW_WGK:ROW_WGK + RANK].reshape(RANK, 2, HW).transpose(1, 0, 2).reshape(2 * RANK, HW)
    g_wgk = lax.dynamic_slice(g_wgk_full, (0, chip * n_lb), (2 * RANK, n_lb))

    small_items = [
        (g_c_ctx, c_ctx.reshape(1, d), m_c_ctx.reshape(1, d), v_c_ctx.reshape(1, d)),
        (g_b_mod, b_mod, m_b_mod, v_b_mod),
        (g_pre1, norm_pre1, m_norm_pre1, v_norm_pre1),
        (g_post1, norm_post1, m_norm_post1, v_norm_post1),
        (g_pre2, norm_pre2, m_norm_pre2, v_norm_pre2),
        (g_post2, norm_post2, m_norm_post2, v_norm_post2),
        (g_hg_lb, hg_lb.reshape(4, n_lb), m_hg_lb.reshape(4, n_lb), v_hg_lb.reshape(4, n_lb)),
        (g_hg_on, hg_onorm, m_hg_onorm, v_hg_onorm),
        (g_wgk, gla_w_gk.reshape(2 * RANK, n_lb), m_gla_w_gk.reshape(2 * RANK, n_lb), v_gla_w_gk.reshape(2 * RANK, n_lb)),
        (g_bgk, gla_b_gk.reshape(2, n_lb), m_gla_b_gk.reshape(2, n_lb), v_gla_b_gk.reshape(2, n_lb)),
        (g_gla_on, gla_onorm, m_gla_onorm, v_gla_onorm),
    ]
    small_res = _adamw_whole(small_items, "adamw_small")
    mod_res = _adamw_tiled(g_w_mod, w_mod[0], m_w_mod[0], v_w_mod[0], "adamw_w_mod")

    loss = lax.psum(loss_part[0, 0], ("x", "y", "c"))

    shapes = dict(c_ctx=c_ctx.shape, b_mod=b_mod.shape, norm_pre1=norm_pre1.shape, norm_post1=norm_post1.shape,
                  norm_pre2=norm_pre2.shape, norm_post2=norm_post2.shape, hg_lb=hg_lb.shape, hg_onorm=hg_onorm.shape,
                  gla_w_gk=gla_w_gk.shape, gla_b_gk=gla_b_gk.shape, gla_onorm=gla_onorm.shape)
    small_names = ["c_ctx", "b_mod", "norm_pre1", "norm_post1", "norm_pre2", "norm_post2", "hg_lb", "hg_onorm",
                   "gla_w_gk", "gla_b_gk", "gla_onorm"]
    grads, deltas, new_m, new_v = {}, {}, {}, {}
    for nm, item, res in zip(small_names, small_items, small_res):
        grads[nm] = item[0].reshape(shapes[nm])
        deltas[nm], new_m[nm], new_v[nm] = (r.reshape(shapes[nm]) for r in res)
    grads["w_mod"] = g_w_mod[None]
    deltas["w_mod"], new_m["w_mod"], new_v["w_mod"] = (r[None] for r in mod_res)
    for nm in names:
        grads[nm], deltas[nm], new_m[nm], new_v[nm] = big[nm]
    order = ["c_ctx", "w_mod", "b_mod", "norm_pre1", "norm_post1", "norm_pre2", "norm_post2", "w_in", "hg_lb",
             "hg_onorm", "gla_w_gk", "gla_b_gk", "gla_onorm", "w_br_hg", "w_br_gla", "w_out", "w_ff_gate", "w_ff_up",
             "w_ff_down"]
    return (loss, grad_x[None], *[grads[n] for n in order], *[deltas[n] for n in order],
            *[new_m[n] for n in order], *[new_v[n] for n in order])


def _weight_grad_cols(xs, dy, n_cols, name, tk=256, tn=512):
    rows = dy.shape[0]
    k = xs.shape[1]

    def body(x_ref, dy_ref, o_ref):
        o_ref[...] = _dot_tn(x_ref[...], dy_ref[...])

    return pl.pallas_call(
        body, name=name, grid=(k // tk, n_cols // tn),
        out_shape=jax.ShapeDtypeStruct((k, n_cols), F32),
        in_specs=[pl.BlockSpec((rows, tk), lambda i, j: (0, i)), pl.BlockSpec((rows, tn), lambda i, j: (0, j))],
        out_specs=pl.BlockSpec((tk, tn), lambda i, j: (i, j)),
        compiler_params=_cparams(dimension_semantics=("parallel", "parallel")),
    )(xs, dy)
```

```python
import functools

import jax
import jax.numpy as jnp
from jax import lax
from jax.experimental import pallas as pl
from jax.experimental.pallas import tpu as pltpu

F32 = jnp.float32
BF16 = jnp.bfloat16
HIGHEST = lax.Precision.HIGHEST
MESH = pl.DeviceIdType.MESH

EPS = 1e-6
CHUNK = 64
SUB = 16
NSUB = CHUNK // SUB
NH = 4
HD = 128
HW = NH * HD
RANK = 16
GATE_NORM = 16.0
N_MOD = 6
TM = 256
TM_FFN = 128
N_DEV = 8
N_CHIP = 4
VMEM_LIMIT = 56 * 1024 * 1024

ADAM_LR = 0.001
ADAM_B1 = 0.9
ADAM_B2 = 0.999
ADAM_EPS = 1e-08
ADAM_WD = 0.01
ADAM_STEP = 10

VMEM_SPEC = pl.BlockSpec(memory_space=pltpu.VMEM)
ANY_SPEC = pl.BlockSpec(memory_space=pl.ANY)
HBM_SPEC = pl.BlockSpec(memory_space=pltpu.HBM)
SEM_SPEC = pl.BlockSpec(memory_space=pltpu.SEMAPHORE)
EFFECT = pltpu.SideEffectType.DATAFLOW_SIDE_EFFECTING


def _cparams(**kw):
    return pltpu.CompilerParams(vmem_limit_bytes=VMEM_LIMIT, **kw)


def _dot(a, b):
    return jnp.dot(a.astype(BF16), b.astype(BF16), preferred_element_type=F32)


def _dot_nt(a, b):
    return lax.dot_general(a.astype(BF16), b.astype(BF16), (((1,), (1,)), ((), ())), preferred_element_type=F32)


def _dot_tn(a, b):
    return lax.dot_general(a.astype(BF16), b.astype(BF16), (((0,), (0,)), ((), ())), preferred_element_type=F32)


def _sigmoid(x):
    return 1.0 / (1.0 + jnp.exp(-x))


def _silu(x):
    return x * _sigmoid(x)


def _dsilu(x):
    s = _sigmoid(x)
    return s * (1.0 + x * (1.0 - s))


def _log_sigmoid(x):
    return jnp.minimum(x, 0.0) - jnp.log(1.0 + jnp.exp(-jnp.abs(x)))


def _colsum(a):
    return jnp.sum(a, axis=0, keepdims=True)


def _rms(a):
    r = lax.rsqrt(jnp.mean(a * a, axis=-1, keepdims=True) + EPS)
    return a * r, r


def _rms_bwd(dn, n, r):
    return r * (dn - n * jnp.mean(dn * n, axis=-1, keepdims=True))


def _place():
    x, y, c = lax.axis_index("x"), lax.axis_index("y"), lax.axis_index("c")
    chips = [(1 - x, y), (x, 1 - y), (1 - x, 1 - y)]
    return x, y, c, chips


def _allgather8(v, name):
    rows, cols = v.shape

    def body(x_ref, out_ref, send_sems, recv_sems, local_sem):
        x, y, c, chips = _place()
        me, sibling = (x, y, c), (x, y, 1 - c)

        def blk(px, py, pc):
            return out_ref.at[4 * px + 2 * py + pc]

        def copy(k, block, to, src=None):
            return pltpu.make_async_remote_copy(
                src_ref=blk(*block) if src is None else src, dst_ref=blk(*block),
                send_sem=send_sems.at[k], recv_sem=recv_sems.at[k], device_id=to, device_id_type=MESH)

        mine = pltpu.make_async_copy(x_ref, blk(*me), local_sem)
        mine.start()
        first = [copy(0, me, sibling, src=x_ref)]
        first += [copy(1 + j, me, (*chip, c), src=x_ref) for j, chip in enumerate(chips)]
        for cp in first:
            cp.start()
        passed = [copy(4 + j, (*chip, c), sibling) for j, chip in enumerate(chips)]
        for j, chip in enumerate(chips):
            copy(1 + j, (*chip, c), me).wait_recv()
            passed[j].start()
        copy(0, sibling, me).wait_recv()
        for j, chip in enumerate(chips):
            copy(4 + j, (*chip, 1 - c), me).wait_recv()
        for cp in first + passed:
            cp.wait_send()
        mine.wait()

    return pl.pallas_call(
        body, name=name,
        out_shape=jax.ShapeDtypeStruct((N_DEV, rows, cols), v.dtype),
        in_specs=[VMEM_SPEC], out_specs=VMEM_SPEC,
        scratch_shapes=[pltpu.SemaphoreType.DMA((7,)), pltpu.SemaphoreType.DMA((7,)), pltpu.SemaphoreType.DMA],
    )(v)


def _gather_shards(shards, name):
    n = len(shards)

    def body(*refs):
        ins, outs = refs[:n], refs[n:2 * n]
        send_sems, recv_sems, local_sems = refs[2 * n:]
        x, y, c, chips = _place()
        me_chip = 2 * x + y
        sibling = (x, y, 1 - c)

        def half(k, chip_id, pc):
            h = shards[k].shape[0] // 2
            return outs[k].at[chip_id, pl.ds(pl.multiple_of(pc * h, 8), h), :]

        def copy(k, j, chip_id, pc, to, src=None):
            return pltpu.make_async_remote_copy(
                src_ref=half(k, chip_id, pc) if src is None else src, dst_ref=half(k, chip_id, pc),
                send_sem=send_sems.at[k, j], recv_sem=recv_sems.at[k, j], device_id=to, device_id_type=MESH)

        locals_ = []
        for k in range(n):
            cp = pltpu.make_async_copy(ins[k], outs[k].at[me_chip], local_sems.at[k])
            cp.start()
            locals_.append(cp)
        started = []
        for k in range(n):
            h = shards[k].shape[0] // 2
            src = ins[k].at[pl.ds(pl.multiple_of(c * h, 8), h), :]
            for j, chip in enumerate(chips):
                cp = copy(k, j, me_chip, c, (*chip, c), src=src)
                cp.start()
                started.append(cp)
        for k in range(n):
            for j, (px, py) in enumerate(chips):
                copy(k, j, 2 * px + py, c, sibling).wait_recv()
                cp = copy(k, 3 + j, 2 * px + py, c, sibling)
                cp.start()
                started.append(cp)
        for k in range(n):
            for j, (px, py) in enumerate(chips):
                copy(k, 3 + j, 2 * px + py, 1 - c, sibling).wait_recv()
        for cp in started:
            cp.wait_send()
        for cp in locals_:
            cp.wait()

    return pl.pallas_call(
        body, name=name,
        out_shape=[jax.ShapeDtypeStruct((N_CHIP,) + s.shape, s.dtype) for s in shards],
        in_specs=[ANY_SPEC] * n, out_specs=[ANY_SPEC] * n,
        scratch_shapes=[pltpu.SemaphoreType.DMA((n, 6)), pltpu.SemaphoreType.DMA((n, 6)),
                        pltpu.SemaphoreType.DMA((n,))],
    )(*shards)


def _hbm(a):
    return pltpu.with_memory_space_constraint(a, pltpu.HBM)


def _gather_start(shards, name):
    n = len(shards)
    n_sem = 3 * n

    def body(*refs):
        ins, lands = refs[:n], refs[n:2 * n]
        send_sems, recv_sems = refs[2 * n:2 * n + n_sem], refs[2 * n + n_sem:2 * n + 2 * n_sem]
        token = refs[-1]
        x, y, c, chips = _place()
        me_chip = 2 * x + y
        for k in range(n):
            h = shards[k].shape[0] // 2
            rows = pl.ds(pl.multiple_of(c * h, 8), h)
            for j, chip in enumerate(chips):
                pltpu.make_async_remote_copy(
                    src_ref=ins[k].at[rows, :], dst_ref=lands[k].at[me_chip, rows, :],
                    send_sem=send_sems[3 * k + j], recv_sem=recv_sems[3 * k + j],
                    device_id=(*chip, c), device_id_type=MESH).start()
        token[...] = jnp.zeros_like(token)

    lands = [_hbm(lax.empty((N_CHIP,) + s.shape, s.dtype)) for s in shards]
    out = pl.pallas_call(
        body, name=name,
        out_shape=(*[pltpu.SemaphoreType.DMA(())] * (2 * n_sem),
                   *[pltpu.HBM(s.shape, s.dtype) for s in shards],
                   *[pltpu.HBM(l.shape, l.dtype) for l in lands],
                   jax.ShapeDtypeStruct((8, 128), F32)),
        in_specs=[HBM_SPEC] * (2 * n),
        out_specs=(*[SEM_SPEC] * (2 * n_sem), *[HBM_SPEC] * (2 * n), VMEM_SPEC),
        input_output_aliases={i: 2 * n_sem + i for i in range(2 * n)},
        compiler_params=pltpu.CompilerParams(has_side_effects=EFFECT),
    )(*[_hbm(s) for s in shards], *lands)
    sems = list(out[:2 * n_sem])
    return sems, list(out[2 * n_sem:2 * n_sem + n]), list(out[2 * n_sem + n:2 * n_sem + 2 * n]), out[-1]


def _gather_wait(sems, shards, lands, after, name):
    n = len(shards)
    n_sem = 3 * n

    def body(*refs):
        ins, lnd = refs[:n], refs[n:2 * n]
        s_sems, r_sems = refs[2 * n:2 * n + n_sem], refs[2 * n + n_sem:2 * n + 2 * n_sem]
        x, y, c, chips = _place()
        for k in range(n):
            h = shards[k].shape[0] // 2
            rows = pl.ds(pl.multiple_of(c * h, 8), h)
            for j, (px, py) in enumerate(chips):
                cp = pltpu.make_async_remote_copy(
                    src_ref=ins[k].at[rows, :], dst_ref=lnd[k].at[2 * px + py, rows, :],
                    send_sem=s_sems[3 * k + j], recv_sem=r_sems[3 * k + j],
                    device_id=(px, py, c), device_id_type=MESH)
                cp.wait_send()
                cp.wait_recv()

    out = pl.pallas_call(
        body, name=name,
        out_shape=(*[pltpu.HBM(s.shape, s.dtype) for s in shards], *[pltpu.HBM(l.shape, l.dtype) for l in lands]),
        in_specs=[HBM_SPEC] * (2 * n) + [SEM_SPEC] * (2 * n_sem) + [ANY_SPEC],
        out_specs=[HBM_SPEC] * (2 * n),
        input_output_aliases={i: i for i in range(2 * n)},
        compiler_params=pltpu.CompilerParams(has_side_effects=EFFECT),
    )(*shards, *lands, *sems, after)
    return list(out[:n]), list(out[n:])


def _gather_finish(shards, lands, name):
    n = len(shards)

    def body(*refs):
        ins, lnd = refs[:n], refs[2 * n:3 * n]
        send_sems, recv_sems, local_sems = refs[3 * n:]
        x, y, c, chips = _place()
        me_chip = 2 * x + y
        sibling = (x, y, 1 - c)

        def half(k, chip_id, pc):
            h = shards[k].shape[0] // 2
            return lnd[k].at[chip_id, pl.ds(pl.multiple_of(pc * h, 8), h), :]

        def copy(k, j, chip_id, pc):
            return pltpu.make_async_remote_copy(
                src_ref=half(k, chip_id, pc), dst_ref=half(k, chip_id, pc),
                send_sem=send_sems.at[k, j], recv_sem=recv_sems.at[k, j], device_id=sibling, device_id_type=MESH)

        locals_, started = [], []
        for k in range(n):
            cp = pltpu.make_async_copy(ins[k], lnd[k].at[me_chip], local_sems.at[k])
            cp.start()
            locals_.append(cp)
            for j, (px, py) in enumerate(chips):
                cp = copy(k, j, 2 * px + py, c)
                cp.start()
                started.append(cp)
        for k in range(n):
            for j, (px, py) in enumerate(chips):
                copy(k, j, 2 * px + py, 1 - c).wait_recv()
        for cp in started:
            cp.wait_send()
        for cp in locals_:
            cp.wait()

    out = pl.pallas_call(
        body, name=name,
        out_shape=[jax.ShapeDtypeStruct(l.shape, l.dtype) for l in lands],
        in_specs=[ANY_SPEC] * (2 * n), out_specs=[ANY_SPEC] * n,
        input_output_aliases={n + i: i for i in range(n)},
        scratch_shapes=[pltpu.SemaphoreType.DMA((n, 3)), pltpu.SemaphoreType.DMA((n, 3)),
                        pltpu.SemaphoreType.DMA((n,))],
    )(*shards, *lands)
    return list(out)


def _send_other_half(arrs, name):
    n = len(arrs)

    def body(*refs):
        ins, outs = refs[:n], refs[n:2 * n]
        send_sems, recv_sems = refs[2 * n:]
        x, y, c, _ = _place()
        cps = []
        for k in range(n):
            h = arrs[k].shape[1] // 2
            cp = pltpu.make_async_remote_copy(
                src_ref=ins[k].at[:, pl.ds(pl.multiple_of((1 - c) * h, 8), h), :], dst_ref=outs[k],
                send_sem=send_sems.at[k], recv_sem=recv_sems.at[k], device_id=(x, y, 1 - c), device_id_type=MESH)
            cp.start()
            cps.append(cp)
        for cp in cps:
            cp.wait()

    return pl.pallas_call(
        body, name=name,
        out_shape=[jax.ShapeDtypeStruct((a.shape[0], a.shape[1] // 2, a.shape[2]), a.dtype) for a in arrs],
        in_specs=[ANY_SPEC] * n, out_specs=[ANY_SPEC] * n,
        scratch_shapes=[pltpu.SemaphoreType.DMA((n,)), pltpu.SemaphoreType.DMA((n,))],
    )(*arrs)


def _blocks_to_owner(arrs, name):
    n = len(arrs)

    def body(*refs):
        ins, outs = refs[:n], refs[n:2 * n]
        send_sems, recv_sems, local_sems = refs[2 * n:]
        x, y, c, chips = _place()
        me_chip = 2 * x + y
        locals_, started = [], []
        for k in range(n):
            cp = pltpu.make_async_copy(ins[k].at[me_chip], outs[k].at[me_chip], local_sems.at[k])
            cp.start()
            locals_.append(cp)

        def copy(k, j, src_block, dst_slot, to):
            return pltpu.make_async_remote_copy(
                src_ref=ins[k].at[src_block], dst_ref=outs[k].at[dst_slot],
                send_sem=send_sems.at[k, j], recv_sem=recv_sems.at[k, j], device_id=to, device_id_type=MESH)

        for k in range(n):
            for j, (px, py) in enumerate(chips):
                cp = copy(k, j, 2 * px + py, me_chip, (px, py, c))
                cp.start()
                started.append(cp)
        for k in range(n):
            for j, (px, py) in enumerate(chips):
                copy(k, j, me_chip, 2 * px + py, (px, py, c)).wait_recv()
        for cp in started:
            cp.wait_send()
        for cp in locals_:
            cp.wait()

    return pl.pallas_call(
        body, name=name,
        out_shape=[jax.ShapeDtypeStruct(a.shape, a.dtype) for a in arrs],
        in_specs=[ANY_SPEC] * n, out_specs=[ANY_SPEC] * n,
        scratch_shapes=[pltpu.SemaphoreType.DMA((n, 3)), pltpu.SemaphoreType.DMA((n, 3)),
                        pltpu.SemaphoreType.DMA((n,))],
    )(*arrs)


def _swap_with_sibling(arrs, name):
    n = len(arrs)

    def body(*refs):
        ins, outs = refs[:n], refs[n:2 * n]
        send_sems, recv_sems = refs[2 * n:]
        x, y, c, _ = _place()
        cps = []
        for k in range(n):
            cp = pltpu.make_async_remote_copy(
                src_ref=ins[k], dst_ref=outs[k], send_sem=send_sems.at[k], recv_sem=recv_sems.at[k],
                device_id=(x, y, 1 - c), device_id_type=MESH)
            cp.start()
            cps.append(cp)
        for cp in cps:
            cp.wait()

    return pl.pallas_call(
        body, name=name,
        out_shape=[jax.ShapeDtypeStruct(a.shape, a.dtype) for a in arrs],
        in_specs=[ANY_SPEC] * n, out_specs=[ANY_SPEC] * n,
        scratch_shapes=[pltpu.SemaphoreType.DMA((n,)), pltpu.SemaphoreType.DMA((n,))],
    )(*arrs)


def _row_tile(h, mult=8, cap=128):
    for t in range(cap - cap % mult, mult - 1, -mult):
        if h % t == 0:
            return t
    raise ValueError(h)


def _cast_bf16(a, name):
    rows, cols = a.shape
    tr = _row_tile(rows, 16, 256)

    def body(a_ref, o_ref):
        o_ref[...] = a_ref[...].astype(BF16)

    return pl.pallas_call(
        body, name=name, grid=(rows // tr,),
        out_shape=jax.ShapeDtypeStruct(a.shape, BF16),
        in_specs=[pl.BlockSpec((tr, cols), lambda i: (i, 0))],
        out_specs=pl.BlockSpec((tr, cols), lambda i: (i, 0)),
        compiler_params=_cparams(dimension_semantics=("parallel",)),
    )(a)


def _pair_sum(c_arr, full, recv, name):
    nb, rows, cols = full.shape
    h = rows // 2
    tr = _row_tile(h, 16, 256)
    steps = h // tr

    def body(c_ref, f_ref, r_ref, o_ref):
        o_ref[...] = (f_ref[...] + r_ref[...]).astype(BF16)

    return pl.pallas_call(
        body, name=name,
        grid_spec=pltpu.PrefetchScalarGridSpec(
            num_scalar_prefetch=1, grid=(nb, steps),
            in_specs=[pl.BlockSpec((1, tr, cols), lambda b, i, c_ref: (b, c_ref[0] * steps + i, 0)),
                      pl.BlockSpec((1, tr, cols), lambda b, i, c_ref: (b, i, 0))],
            out_specs=pl.BlockSpec((1, tr, cols), lambda b, i, c_ref: (b, i, 0))),
        out_shape=jax.ShapeDtypeStruct((nb, h, cols), BF16),
        compiler_params=_cparams(dimension_semantics=("parallel", "parallel")),
    )(c_arr, full, recv)


def _sum_chips(got, name):
    nb, h, cols = got.shape
    tr = _row_tile(h, 16, 256)

    def body(g_ref, o_ref):
        g = g_ref[...].astype(F32)
        o_ref[...] = ((g[0] + g[1]) + g[2]) + g[3]

    return pl.pallas_call(
        body, name=name, grid=(h // tr,),
        out_shape=jax.ShapeDtypeStruct((h, cols), F32),
        in_specs=[pl.BlockSpec((nb, tr, cols), lambda i: (0, i, 0))],
        out_specs=pl.BlockSpec((tr, cols), lambda i: (i, 0)),
        compiler_params=_cparams(dimension_semantics=("parallel",)),
    )(got)


def _adam_math(g, w, m, v):
    m1 = ADAM_B1 * m + (1.0 - ADAM_B1) * g
    v1 = ADAM_B2 * v + (1.0 - ADAM_B2) * (g * g)
    m_hat = m1 / (1.0 - ADAM_B1 ** ADAM_STEP)
    v_hat = v1 / (1.0 - ADAM_B2 ** ADAM_STEP)
    delta = -ADAM_LR * (m_hat / (jnp.sqrt(v_hat) + ADAM_EPS) + ADAM_WD * w)
    return delta, m1, v1


def _adamw_halves(c_arr, own, other, w, m, v, name):
    rows, cols = w.shape
    h = rows // 2
    tr = _row_tile(h)
    steps = h // tr

    def body(c_ref, own_ref, oth_ref, w_ref, m_ref, v_ref, g_out, d_out, m_out, v_out):
        g = jnp.where(pl.program_id(0) == c_ref[0], own_ref[...], oth_ref[...])
        d, m1, v1 = _adam_math(g, w_ref[...], m_ref[...], v_ref[...])
        g_out[...] = g
        d_out[...] = d
        m_out[...] = m1
        v_out[...] = v1

    half_spec = pl.BlockSpec((tr, cols), lambda p, i, c_ref: (i, 0))
    full_spec = pl.BlockSpec((tr, cols), lambda p, i, c_ref: (p * steps + i, 0))
    return pl.pallas_call(
        body, name=name,
        grid_spec=pltpu.PrefetchScalarGridSpec(
            num_scalar_prefetch=1, grid=(2, steps),
            in_specs=[half_spec, half_spec, full_spec, full_spec, full_spec],
            out_specs=[full_spec] * 4),
        out_shape=[jax.ShapeDtypeStruct(w.shape, F32)] * 4,
        compiler_params=_cparams(dimension_semantics=("parallel", "parallel")),
    )(c_arr, own, other, w, m, v)


def _adamw_whole(items, name):
    n = len(items)

    def body(*refs):
        ins, outs = refs[:4 * n], refs[4 * n:]
        for k in range(n):
            g, w, m, v = (r[...] for r in ins[4 * k:4 * k + 4])
            d, m1, v1 = _adam_math(g, w, m, v)
            outs[3 * k][...] = d
            outs[3 * k + 1][...] = m1
            outs[3 * k + 2][...] = v1

    flat = [a for it in items for a in it]
    shapes = [jax.ShapeDtypeStruct(it[1].shape, F32) for it in items for _ in range(3)]
    out = pl.pallas_call(
        body, name=name, out_shape=shapes,
        in_specs=[VMEM_SPEC] * (4 * n), out_specs=[VMEM_SPEC] * (3 * n),
        compiler_params=_cparams(),
    )(*flat)
    return [tuple(out[3 * k:3 * k + 3]) for k in range(n)]


def _adamw_tiled(g, w, m, v, name):
    rows, cols = w.shape
    tr = _row_tile(rows)

    def body(g_ref, w_ref, m_ref, v_ref, d_out, m_out, v_out):
        d, m1, v1 = _adam_math(g_ref[...], w_ref[...], m_ref[...], v_ref[...])
        d_out[...] = d
        m_out[...] = m1
        v_out[...] = v1

    spec = pl.BlockSpec((tr, cols), lambda i: (i, 0))
    return pl.pallas_call(
        body, name=name, grid=(rows // tr,),
        out_shape=[jax.ShapeDtypeStruct(w.shape, F32)] * 3,
        in_specs=[spec] * 4, out_specs=[spec] * 3,
        compiler_params=_cparams(dimension_semantics=("parallel",)),
    )(g, w, m, v)


def _mod_forward(cond, w_mod, b_mod_cols, name):
    def body(c_ref, w_ref, b_ref, o_ref):
        o_ref[...] = _dot(_silu(c_ref[...]), w_ref[...]) + b_ref[...]

    return pl.pallas_call(
        body, name=name, out_shape=jax.ShapeDtypeStruct((cond.shape[0], w_mod.shape[1]), F32),
        in_specs=[VMEM_SPEC] * 3, out_specs=VMEM_SPEC, compiler_params=_cparams(),
    )(cond, w_mod, b_mod_cols)


def _mod_backward(cond, w_mod, dmod_cols, name):
    def body(c_ref, w_ref, d_ref, gw_ref, gc_ref):
        s = _silu(c_ref[...])
        d = d_ref[...]
        gw_ref[...] = _dot_tn(s, d)
        gc_ref[...] = _dot_nt(d[8:16, :], w_ref[...])

    return pl.pallas_call(
        body, name=name,
        out_shape=[jax.ShapeDtypeStruct(w_mod.shape, F32), jax.ShapeDtypeStruct((8, w_mod.shape[0]), F32)],
        in_specs=[VMEM_SPEC] * 3, out_specs=[VMEM_SPEC] * 2, compiler_params=_cparams(),
    )(cond, w_mod, dmod_cols)


def _col_chunks(width, step=512):
    return [(s, min(step, width - s)) for s in range(0, width, step)]


def _in_projection(z, modc, modx, pre1, w_r, n_ctx_tiles, name):
    rows, d = z.shape
    width = w_r.shape[1]

    def body(z_ref, modc_ref, modx_ref, pre_ref, w_ref, h_ref, p_ref):
        is_ctx = pl.program_id(0) < n_ctx_tiles
        n, _ = _rms(z_ref[...])
        shift = jnp.where(is_ctx, modc_ref[0:1, :], modx_ref[0:1, :])
        scale = jnp.where(is_ctx, modc_ref[1:2, :], modx_ref[1:2, :])
        h = (n * pre_ref[...] * (1.0 + scale) + shift).astype(BF16)
        h_ref[...] = h
        for s, w in _col_chunks(width):
            p_ref[:, s:s + w] = jnp.dot(h, w_ref[:, s:s + w], preferred_element_type=F32)

    row = lambda i: (i, 0)
    fixed = lambda i: (0, 0)
    return pl.pallas_call(
        body, name=name, grid=(rows // TM,),
        out_shape=[jax.ShapeDtypeStruct((rows, d), BF16), jax.ShapeDtypeStruct((rows, width), F32)],
        in_specs=[pl.BlockSpec((TM, d), row), pl.BlockSpec((8, d), fixed), pl.BlockSpec((8, d), fixed),
                  pl.BlockSpec((1, d), fixed), VMEM_SPEC],
        out_specs=[pl.BlockSpec((TM, d), row), pl.BlockSpec((TM, width), row)],
        compiler_params=_cparams(dimension_semantics=("parallel",)),
    )(z, modc, modx, pre1, w_r)


C_HQ, C_HI, C_HF_FW, C_HF_BW, C_HGATE, C_GQ, C_GK, C_GV, C_GGATE = range(9)
OFF_GATE_HG = 9 * HW
OFF_LR = 13 * HW
P_WIDTH = OFF_LR + 128


def _head_norm_fwd(o, w):
    outs, ns, rs = [], [], []
    for h in range(NH):
        n, r = _rms(o[:, h * HD:(h + 1) * HD])
        ns.append(n)
        rs.append(r)
        outs.append(n * w)
    return jnp.concatenate(outs, axis=1), ns, rs


def _mixer_tail(z, o_hg, o_gla, p_hgate, p_ggate, p_gate_hg, p_gate_gla, hg_on, gla_on, wbh, wbg, wout):
    on_hg, n_hg, r_hg = _head_norm_fwd(o_hg, hg_on)
    on_gla, n_gla, r_gla = _head_norm_fwd(o_gla, gla_on)
    og_hg = (on_hg * _silu(p_hgate)).astype(BF16)
    og_gla = (on_gla * _silu(p_ggate)).astype(BF16)
    b_hg = jnp.dot(og_hg, wbh, preferred_element_type=F32)
    b_gla = jnp.dot(og_gla, wbg, preferred_element_type=F32)
    s_hg = _sigmoid(p_gate_hg)
    s_gla = _sigmoid(p_gate_gla)
    merged = (s_hg * b_hg + s_gla * b_gla).astype(BF16)
    y1 = jnp.dot(merged, wout, preferred_element_type=F32)
    return dict(on_hg=on_hg, n_hg=n_hg, r_hg=r_hg, on_gla=on_gla, n_gla=n_gla, r_gla=r_gla, og_hg=og_hg,
                og_gla=og_gla, b_hg=b_hg, b_gla=b_gla, s_hg=s_hg, s_gla=s_gla, merged=merged, y1=y1)


def _mixer_ffn(x_lat, p, o_list, modx, norms, onorms, w_br_hg, w_br_gla, w_out, w_gate, w_up, w_down, target,
               n_ctx_tiles, name):
    rows, d = x_lat.shape
    dff = w_gate.shape[1]
    inv_d = 1.0 / d

    def body(x_ref, ofw_hg, obw_hg, ofw_gla, obw_gla, p_hgate, p_ggate, p_ghg_a, p_ghg_b, p_ggla_a, p_ggla_b,
             modx_ref, norm_ref, on_ref, wbh_ref, wbg_ref, wout_ref, wg_ref, wu_ref, wd_ref, t_ref,
             loss_ref, dz2_ref, y1_ref, mrg_ref, oghg_ref, oggla_ref, h2_ref, a_ref, du_ref, dv_ref, dy2_ref,
             stat_ref):
        i = pl.program_id(0)
        post1, pre2, post2 = norm_ref[1:2, :], norm_ref[2:3, :], norm_ref[3:4, :]
        gate1, shift2, scale2, gate2 = modx_ref[2:3, :], modx_ref[3:4, :], modx_ref[4:5, :], modx_ref[5:6, :]
        p_gate_hg = jnp.concatenate([p_ghg_a[...], p_ghg_b[...]], axis=1)
        p_gate_gla = jnp.concatenate([p_ggla_a[...], p_ggla_b[...]], axis=1)
        t = _mixer_tail(x_ref[...], ofw_hg[...] + obw_hg[...], ofw_gla[...] + obw_gla[...], p_hgate[...],
                        p_ggate[...], p_gate_hg, p_gate_gla, on_ref[0:1, 0:HD], on_ref[1:2, 0:HD],
                        wbh_ref[...], wbg_ref[...], wout_ref[...])
        y1_ref[...] = t["y1"]
        mrg_ref[...] = t["merged"]
        oghg_ref[...] = t["og_hg"]
        oggla_ref[...] = t["og_gla"]
        n1, _ = _rms(t["y1"])
        z2 = x_ref[...] + n1 * post1 * gate1
        n2, r2 = _rms(z2)
        nw2 = n2 * pre2
        h2 = (nw2 * (1.0 + scale2) + shift2).astype(BF16)
        h2_ref[...] = h2
        u = jnp.dot(h2, wg_ref[...], preferred_element_type=F32)
        v = jnp.dot(h2, wu_ref[...], preferred_element_type=F32)
        su = _silu(u)
        a = (su * v).astype(BF16)
        a_ref[...] = a
        y2 = jnp.dot(a, wd_ref[...], preferred_element_type=F32)
        n3, r3 = _rms(y2)
        z3 = z2 + n3 * post2 * gate2
        err = z3 - t_ref[...]
        part = 0.5 * inv_d * jnp.sum(err * err)
        dz3 = err * inv_d
        dgate2 = _colsum(dz3 * n3 * post2)
        tt = dz3 * gate2
        dpost2 = _colsum(tt * n3)
        dy2 = _rms_bwd(tt * post2, n3, r3).astype(BF16)
        dy2_ref[...] = dy2
        da = _dot_nt(dy2, wd_ref[...])
        du = (da * v * _dsilu(u)).astype(BF16)
        dv = (da * su).astype(BF16)
        du_ref[...] = du
        dv_ref[...] = dv
        dh2 = _dot_nt(du, wg_ref[...]) + _dot_nt(dv, wu_ref[...])
        dshift2 = _colsum(dh2)
        dscale2 = _colsum(dh2 * nw2)
        dnw2 = dh2 * (1.0 + scale2)
        dpre2 = _colsum(dnw2 * n2)
        dz2_ref[...] = dz3 + _rms_bwd(dnw2 * pre2, n2, r2)

        @pl.when(i == 0)
        def _():
            stat_ref[...] = jnp.zeros_like(stat_ref)
            loss_ref[...] = jnp.zeros_like(loss_ref)

        for r, val in enumerate((dshift2, dscale2, dgate2, dpre2, dpost2)):
            stat_ref[r:r + 1, :] += val
        loss_ref[...] += part

    tm = TM_FFN
    ctx_tiles = n_ctx_tiles * (TM // tm)
    lat = lambda i: (i, 0)
    full = lambda i: (i + ctx_tiles, 0)
    fixed = lambda i: (0, 0)

    def pcol(blk):
        return pl.BlockSpec((tm, HW), lambda i: (i + ctx_tiles, blk))

    in_specs = ([pl.BlockSpec((tm, d), lat)] + [pl.BlockSpec((tm, HW), full)] * 4
                + [pcol(C_HGATE), pcol(C_GGATE), pcol(9), pcol(10), pcol(11), pcol(12)]
                + [pl.BlockSpec((8, d), fixed), pl.BlockSpec((8, d), fixed), pl.BlockSpec((8, d), fixed)]
                + [VMEM_SPEC] * 6 + [pl.BlockSpec((tm, d), lat)])
    bf = lambda w: jax.ShapeDtypeStruct((rows, w), BF16)
    out_shape = [jax.ShapeDtypeStruct((8, 128), F32), jax.ShapeDtypeStruct((rows, d), F32),
                 jax.ShapeDtypeStruct((rows, d), F32), bf(d), bf(HW), bf(HW), bf(d), bf(dff), bf(dff), bf(dff), bf(d),
                 jax.ShapeDtypeStruct((8, d), F32)]
    out_specs = [pl.BlockSpec((8, 128), fixed), pl.BlockSpec((tm, d), lat), pl.BlockSpec((tm, d), lat),
                 pl.BlockSpec((tm, d), lat), pl.BlockSpec((tm, HW), lat), pl.BlockSpec((tm, HW), lat),
                 pl.BlockSpec((tm, d), lat), pl.BlockSpec((tm, dff), lat), pl.BlockSpec((tm, dff), lat),
                 pl.BlockSpec((tm, dff), lat), pl.BlockSpec((tm, d), lat), pl.BlockSpec((8, d), fixed)]
    return pl.pallas_call(
        body, name=name, grid=(rows // tm,), out_shape=out_shape, in_specs=in_specs, out_specs=out_specs,
        compiler_params=_cparams(dimension_semantics=("arbitrary",)),
    )(x_lat, *o_list, p, p, p, p, p, p, modx, norms, onorms, w_br_hg, w_br_gla, w_out, w_gate, w_up, w_down, target)


def _mixer_tail_bwd(x_lat, p, o_list, dz2, y1, modx, norms, onorms, w_br_hg, w_br_gla, w_out, n_ctx_tiles, n_tiles,
                    name):
    rows, d = x_lat.shape
    total = n_tiles * TM

    def body(x_ref, ofw_hg, obw_hg, ofw_gla, obw_gla, p_hgate, p_ggate, p_ghg_a, p_ghg_b, p_ggla_a, p_ggla_b,
             dz2_ref, y1_ref, modx_ref, norm_ref, on_ref, wbh_ref, wbg_ref, wout_ref,
             dohg_ref, dogla_ref, dhgate_ref, dggate_ref, dghg_ref, dggla_ref, dy1_ref, dbhg_ref, dbgla_ref,
             stat_ref):
        i = pl.program_id(0)

        @pl.when(i == 0)
        def _():
            stat_ref[...] = jnp.zeros_like(stat_ref)

        @pl.when(i < n_ctx_tiles)
        def _():
            for ref in (dohg_ref, dogla_ref, dhgate_ref, dggate_ref, dghg_ref, dggla_ref):
                ref[...] = jnp.zeros_like(ref)

        @pl.when(i >= n_ctx_tiles)
        def _():
            post1, gate1 = norm_ref[1:2, :], modx_ref[2:3, :]
            hg_on, gla_on = on_ref[0:1, 0:HD], on_ref[1:2, 0:HD]
            p_gate_hg = jnp.concatenate([p_ghg_a[...], p_ghg_b[...]], axis=1)
            p_gate_gla = jnp.concatenate([p_ggla_a[...], p_ggla_b[...]], axis=1)
            ph, pg = p_hgate[...], p_ggate[...]
            t = _mixer_tail(x_ref[...], ofw_hg[...] + obw_hg[...], ofw_gla[...] + obw_gla[...], ph, pg,
                            p_gate_hg, p_gate_gla, hg_on, gla_on, wbh_ref[...], wbg_ref[...], wout_ref[...])
            dz2 = dz2_ref[...]
            n1, r1 = _rms(y1_ref[...])
            dgate1 = _colsum(dz2 * n1 * post1)
            tt = dz2 * gate1
            dpost1 = _colsum(tt * n1)
            dy1 = _rms_bwd(tt * post1, n1, r1).astype(BF16)
            dy1_ref[...] = dy1
            dmerged = _dot_nt(dy1, wout_ref[...])
            dghg_ref[...] = dmerged * t["b_hg"] * t["s_hg"] * (1.0 - t["s_hg"])
            dggla_ref[...] = dmerged * t["b_gla"] * t["s_gla"] * (1.0 - t["s_gla"])
            db_hg = (dmerged * t["s_hg"]).astype(BF16)
            db_gla = (dmerged * t["s_gla"]).astype(BF16)
            dbhg_ref[...] = db_hg
            dbgla_ref[...] = db_gla
            don_acc = []
            for (db, wb, pgate, on, ns, rs, gain, gate_ref, do_ref) in (
                    (db_hg, wbh_ref, ph, t["on_hg"], t["n_hg"], t["r_hg"], hg_on, dhgate_ref, dohg_ref),
                    (db_gla, wbg_ref, pg, t["on_gla"], t["n_gla"], t["r_gla"], gla_on, dggate_ref, dogla_ref)):
                dog = _dot_nt(db, wb[...])
                gate_ref[...] = dog * on * _dsilu(pgate)
                don = dog * _silu(pgate)
                acc = jnp.zeros((1, HD), F32)
                for h in range(NH):
                    sl = slice(h * HD, (h + 1) * HD)
                    acc = acc + _colsum(don[:, sl] * ns[h])
                    do_ref[:, sl] = _rms_bwd(don[:, sl] * gain, ns[h], rs[h])
                don_acc.append(acc)
            stat_ref[0:1, :] += dgate1
            stat_ref[1:2, :] += dpost1
            stat_ref[2:3, 0:HD] += don_acc[0]
            stat_ref[2:3, HD:2 * HD] += don_acc[1]

    lat = lambda i: (jnp.maximum(i - n_ctx_tiles, 0), 0)
    full = lambda i: (i, 0)
    fixed = lambda i: (0, 0)

    def pcol(blk):
        return pl.BlockSpec((TM, HW), lambda i: (i, blk))

    in_specs = ([pl.BlockSpec((TM, d), lat)] + [pl.BlockSpec((TM, HW), full)] * 4
                + [pcol(C_HGATE), pcol(C_GGATE), pcol(9), pcol(10), pcol(11), pcol(12)]
                + [pl.BlockSpec((TM, d), lat), pl.BlockSpec((TM, d), lat)]
                + [pl.BlockSpec((8, d), fixed)] * 3 + [VMEM_SPEC] * 3)
    f = lambda w: jax.ShapeDtypeStruct((total, w), F32)
    out_shape = [f(HW), f(HW), f(HW), f(HW), f(d), f(d), jax.ShapeDtypeStruct((rows, d), BF16),
                 jax.ShapeDtypeStruct((rows, d), BF16), jax.ShapeDtypeStruct((rows, d), BF16),
                 jax.ShapeDtypeStruct((8, d), F32)]
    out_specs = ([pl.BlockSpec((TM, HW), full)] * 4 + [pl.BlockSpec((TM, d), full)] * 2
                 + [pl.BlockSpec((TM, d), lat)] * 3 + [pl.BlockSpec((8, d), fixed)])
    return pl.pallas_call(
        body, name=name, grid=(n_tiles,), out_shape=out_shape, in_specs=in_specs, out_specs=out_specs,
        compiler_params=_cparams(dimension_semantics=("arbitrary",)),
    )(x_lat, *o_list, p, p, p, p, p, p, dz2, y1, modx, norms, onorms, w_br_hg, w_br_gla, w_out)


def _in_projection_bwd(z, dz2, modc, modx, pre1, w_r, pieces, n_ctx_tiles, name):
    rows, d = z.shape
    lat_rows = dz2.shape[0]
    width = w_r.shape[1]
    n_pieces = len(pieces)

    def body(*refs):
        z_ref, dz2_ref, modc_ref, modx_ref, pre_ref, w_ref = refs[:6]
        (dhq_f, dhq_b, dhi_f, dhi_b, dhf_f, dhf_b, dhgate, dgq_f, dgq_b, dgk_f, dgk_b, dgv_f, dgv_b, dggate,
         dghg, dggla, dlr_f, dlr_b) = refs[6:6 + n_pieces]
        dp_ref, gx_ref, stat_ref = refs[6 + n_pieces:]
        i = pl.program_id(0)
        is_ctx = i < n_ctx_tiles
        sections = [
            (0, dhq_f[...] + dhq_b[...]), (HW, dhi_f[...] + dhi_b[...]), (2 * HW, dhf_f[...]), (3 * HW, dhf_b[...]),
            (4 * HW, dhgate[...]), (5 * HW, dgq_f[...] + dgq_b[...]), (6 * HW, dgk_f[...] + dgk_b[...]),
            (7 * HW, dgv_f[...] + dgv_b[...]), (8 * HW, dggate[...]),
            (9 * HW, dghg[:, 0:HW]), (10 * HW, dghg[:, HW:2 * HW]),
            (11 * HW, dggla[:, 0:HW]), (12 * HW, dggla[:, HW:2 * HW]), (OFF_LR, dlr_f[...] + dlr_b[...])]
        dh = jnp.zeros((TM, d), F32)
        for off, val in sections:
            w = val.shape[1]
            vb = val.astype(BF16)
            dp_ref[:, off:off + w] = vb
            dh = dh + _dot_nt(vb, w_ref[:, off:off + w])
        n, r = _rms(z_ref[...])
        pre = pre_ref[...]
        scale = jnp.where(is_ctx, modc_ref[1:2, :], modx_ref[1:2, :])
        nw = n * pre
        dshift = _colsum(dh)
        dscale = _colsum(dh * nw)
        dnw = dh * (1.0 + scale)
        dpre = _colsum(dnw * n)
        gx_ref[...] = dz2_ref[...] + _rms_bwd(dnw * pre, n, r)
        zero = jnp.zeros((1, d), F32)

        @pl.when(i == 0)
        def _():
            stat_ref[...] = jnp.zeros_like(stat_ref)

        stat_ref[0:1, :] += jnp.where(is_ctx, zero, dshift)
        stat_ref[1:2, :] += jnp.where(is_ctx, zero, dscale)
        stat_ref[2:3, :] += jnp.where(is_ctx, dshift, zero)
        stat_ref[3:4, :] += jnp.where(is_ctx, dscale, zero)
        stat_ref[4:5, :] += dpre

    full = lambda i: (i, 0)
    lat = lambda i: (jnp.maximum(i - n_ctx_tiles, 0), 0)
    fixed = lambda i: (0, 0)
    piece_specs = [pl.BlockSpec((TM, a.shape[1]), full) for a in pieces]
    in_specs = [pl.BlockSpec((TM, d), full), pl.BlockSpec((TM, d), lat), pl.BlockSpec((8, d), fixed),
                pl.BlockSpec((8, d), fixed), pl.BlockSpec((1, d), fixed), VMEM_SPEC] + piece_specs
    return pl.pallas_call(
        body, name=name, grid=(rows // TM,),
        out_shape=[jax.ShapeDtypeStruct((rows, width), BF16), jax.ShapeDtypeStruct((lat_rows, d), F32),
                   jax.ShapeDtypeStruct((8, d), F32)],
        in_specs=in_specs,
        out_specs=[pl.BlockSpec((TM, width), full), pl.BlockSpec((TM, d), lat), pl.BlockSpec((8, d), fixed)],
        compiler_params=_cparams(dimension_semantics=("arbitrary",)),
    )(z, dz2, modc, modx, pre1, w_r, *pieces)


def _weight_grad(xs, dy, name, col_block=None, tk=256, tn=512):
    rows = dy.shape[0]
    k = xs.shape[1]
    if col_block is None:
        n, tn_, cb = dy.shape[1], tn, 0
    else:
        n, tn_, cb = col_block[0], col_block[0], col_block[1]
    tn_ = min(tn_, n)
    tk_ = min(tk, k)

    def body(x_ref, dy_ref, o_ref):
        o_ref[...] = _dot_tn(x_ref[...], dy_ref[...])

    return pl.pallas_call(
        body, name=name, grid=(k // tk_, n // tn_),
        out_shape=jax.ShapeDtypeStruct((k, n), F32),
        in_specs=[pl.BlockSpec((rows, tk_), lambda i, j: (0, i)),
                  pl.BlockSpec((rows, tn_), lambda i, j: (0, j + cb))],
        out_specs=pl.BlockSpec((tk_, tn_), lambda i, j: (i, j)),
        compiler_params=_cparams(dimension_semantics=("parallel", "parallel")),
    )(xs, dy)


def _chunk_terms(q, k, g, fw):
    c = CHUNK
    r = lax.broadcasted_iota(jnp.int32, (c, c), 0)
    s = lax.broadcasted_iota(jnp.int32, (c, c), 1)
    causal = (s <= r) if fw else (s >= r)
    causal_t = (s >= r) if fw else (s <= r)
    cum = jnp.dot(causal.astype(F32), g, precision=HIGHEST, preferred_element_type=F32)
    row = lax.broadcasted_iota(jnp.int32, (c, 1), 0)
    pos = row if fw else (c - 1 - row)
    starts = [None]
    for j in range(1, NSUB):
        rj = SUB * j - 1 if fw else c - SUB * j
        starts.append(cum[rj:rj + 1, :])
    in_blk = [(pos >= SUB * j) & (pos < SUB * (j + 1)) for j in range(NSUB)]
    e = [jnp.exp(cum)]
    for j in range(1, NSUB):
        e.append(jnp.exp(jnp.where(pos >= SUB * j, cum - starts[j], -1e30)))
    own = jnp.zeros_like(cum)
    for j in range(1, NSUB):
        own = own + jnp.where(in_blk[j], starts[j], 0.0)
    kscale = jnp.exp(own - cum)
    rend = c - 1 if fw else 0
    cend = cum[rend:rend + 1, :]
    tail = jnp.exp(cend - cum)
    qcat = jnp.concatenate([q * e[j] for j in range(NSUB)], axis=1).astype(BF16)
    kt = k * kscale
    km = jnp.concatenate([jnp.where(in_blk[j], kt, 0.0) for j in range(NSUB)], axis=1).astype(BF16)
    return dict(causal=causal, causal_t=causal_t, e=e, in_blk=in_blk, kscale=kscale, cend=cend, tail=tail,
                qcat=qcat, km=km)


def _chunk_fwd(q, k, v, g, st0, fw):
    t = _chunk_terms(q, k, g, fw)
    a = jnp.where(t["causal"], _dot_nt(t["qcat"], t["km"]), 0.0)
    o = _dot(a, v) + _dot_nt(t["qcat"][:, 0:HD], st0)
    st1 = st0 * jnp.exp(t["cend"]) + _dot_tn(v, k * t["tail"])
    return o, st1


def _chunk_bwd(q, k, v, g, st0, do, dst1, fw):
    t = _chunk_terms(q, k, g, fw)
    qcat, km, e = t["qcat"], t["km"], t["e"]
    a_t = jnp.where(t["causal_t"], _dot_nt(km, qcat), 0.0)
    ktail = k * t["tail"]
    dv = _dot(a_t, do) + _dot_nt(ktail, dst1)
    da = jnp.where(t["causal"], _dot_nt(do, v), 0.0)
    da_t = jnp.where(t["causal_t"], _dot_nt(v, do), 0.0)
    dqcat = _dot(da, km)
    dq_inter = e[0] * _dot(do, st0)
    dq = dq_inter
    for j in range(NSUB):
        dq = dq + e[j] * dqcat[:, j * HD:(j + 1) * HD]
    dkm = _dot(da_t, qcat)
    dkt = jnp.zeros_like(k)
    for j in range(NSUB):
        dkt = dkt + jnp.where(t["in_blk"][j], dkm[:, j * HD:(j + 1) * HD], 0.0)
    dk_inter = _dot(v, dst1) * t["tail"]
    dk = dkt * t["kscale"] + dk_inter
    dcum = q * dq_inter - k * dk_inter
    for j in range(NSUB):
        sl = slice(j * HD, (j + 1) * HD)
        dcum = dcum + qcat[:, sl].astype(F32) * dqcat[:, sl] - km[:, sl].astype(F32) * dkm[:, sl]
    ecend = jnp.exp(t["cend"])
    end = ecend * _colsum(st0 * dst1) + _colsum(k * dk_inter)
    dg = jnp.dot(t["causal_t"].astype(F32), dcum, precision=HIGHEST, preferred_element_type=F32) + end
    dst0 = dst1 * ecend + _dot_tn(do, q * e[0])
    return dq, dk, dv, dg, dst0


def _chunk_index(step, n_ctx_chunks, n_chunks, fw):
    if fw:
        return step
    return jnp.where(step < n_ctx_chunks, n_ctx_chunks - 1 - step, n_chunks - 1 + n_ctx_chunks - step)


def _hg_inputs(hq, hf, lbv, d_idx, sl):
    lb = _sigmoid(lbv[d_idx:d_idx + 1, sl] - lbv[2 + d_idx:3 + d_idx, sl])
    sg = _sigmoid(hf)
    f = lb + (1.0 - lb) * sg
    return _silu(hq), 1.0 - f, jnp.log(f), f, sg, lb


def _scan_fwd(p, side, n_ctx_chunks, fw, branch, name):
    rows = p.shape[0]
    n_chunks = rows // CHUNK
    d_idx = 0 if fw else 1
    hg = branch == "hg"
    cols = (C_HQ, C_HI, C_HF_FW + d_idx) if hg else (C_GQ, C_GK, C_GV)

    def body(*refs):
        if hg:
            a_ref, b_ref, c_ref, lb_ref, o_ref, st_ref, state = refs
        else:
            a_ref, b_ref, c_ref, lr_ref, wgk_ref, bgk_ref, o_ref, st_ref, state = refs
            logits = _dot(lr_ref[...], wgk_ref[...]) + bgk_ref[...]
            g_all = _log_sigmoid(logits) * (1.0 / GATE_NORM)

        @pl.when(pl.program_id(0) == 0)
        def _():
            state[...] = jnp.zeros_like(state)

        for h in range(NH):
            sl = slice(h * HD, (h + 1) * HD)
            if hg:
                q, k, g, _, _, _ = _hg_inputs(a_ref[:, sl], c_ref[:, sl], lb_ref[...], d_idx, sl)
                v = b_ref[:, sl]
            else:
                q, k, v, g = a_ref[:, sl] * (HD ** -0.5), b_ref[:, sl], c_ref[:, sl], g_all[:, sl]
            st0 = state[h]
            st_ref[0, h] = st0
            o, st1 = _chunk_fwd(q, k, v, g, st0, fw)
            o_ref[:, sl] = o
            state[h] = st1

    def cmap(blk):
        return pl.BlockSpec((CHUNK, HW), lambda j: (_chunk_index(j, n_ctx_chunks, n_chunks, fw), blk))

    fixed = lambda j: (0, 0)
    in_specs = [cmap(cols[0]), cmap(cols[1]), cmap(cols[2])]
    if hg:
        in_specs += [pl.BlockSpec((4, HW), fixed)]
        args = (p, p, p, side)
    else:
        in_specs += [pl.BlockSpec((CHUNK, 128), lambda j: (_chunk_index(j, n_ctx_chunks, n_chunks, fw), OFF_LR // 128)),
                     pl.BlockSpec((128, HW), fixed), pl.BlockSpec((1, HW), fixed)]
        args = (p, p, p, p, side[0], side[1])
    return pl.pallas_call(
        body, name=name, grid=(n_chunks,),
        out_shape=[jax.ShapeDtypeStruct((rows, HW), F32), jax.ShapeDtypeStruct((n_chunks, NH, HD, HD), F32)],
        in_specs=in_specs,
        out_specs=[pl.BlockSpec((CHUNK, HW), lambda j: (_chunk_index(j, n_ctx_chunks, n_chunks, fw), 0)),
                   pl.BlockSpec((1, NH, HD, HD), lambda j: (_chunk_index(j, n_ctx_chunks, n_chunks, fw), 0, 0, 0))],
        scratch_shapes=[pltpu.VMEM((NH, HD, HD), F32)],
        compiler_params=_cparams(dimension_semantics=("arbitrary",)),
    )(*args)


def _scan_bwd(p, side, states, d_o, n_ctx_chunks, fw, branch, name):
    rows = p.shape[0]
    n_chunks = rows // CHUNK
    d_idx = 0 if fw else 1
    hg = branch == "hg"
    cols = (C_HQ, C_HI, C_HF_FW + d_idx) if hg else (C_GQ, C_GK, C_GV)

    def body(*refs):
        if hg:
            a_ref, b_ref, c_ref, lb_ref, st_ref, do_ref, da_ref, db_ref, dc_ref, dlb_ref, dstate = refs
        else:
            (a_ref, b_ref, c_ref, lr_ref, wgk_ref, bgk_ref, st_ref, do_ref, da_ref, db_ref, dc_ref, dlr_ref,
             dwgk_ref, dbias_ref, dstate) = refs
            lr = lr_ref[...]
            logits = _dot(lr, wgk_ref[...]) + bgk_ref[...]
            g_all = _log_sigmoid(logits) * (1.0 / GATE_NORM)

        @pl.when(pl.program_id(0) == 0)
        def _():
            dstate[...] = jnp.zeros_like(dstate)
            if hg:
                dlb_ref[...] = jnp.zeros_like(dlb_ref)
            else:
                dwgk_ref[...] = jnp.zeros_like(dwgk_ref)
                dbias_ref[...] = jnp.zeros_like(dbias_ref)

        dg_parts = []
        for h in range(NH):
            sl = slice(h * HD, (h + 1) * HD)
            if hg:
                hq, hf = a_ref[:, sl], c_ref[:, sl]
                q, k, g, f, sg, lb = _hg_inputs(hq, hf, lb_ref[...], d_idx, sl)
                v = b_ref[:, sl]
            else:
                q, k, v, g = a_ref[:, sl] * (HD ** -0.5), b_ref[:, sl], c_ref[:, sl], g_all[:, sl]
            dq, dk, dv, dg, dst0 = _chunk_bwd(q, k, v, g, st_ref[0, h], do_ref[:, sl], dstate[h], fw)
            dstate[h] = dst0
            if hg:
                da_ref[:, sl] = dq * _dsilu(hq)
                db_ref[:, sl] = dv
                df = dg / f - dk
                dc_ref[:, sl] = df * (1.0 - lb) * sg * (1.0 - sg)
                dlb_ref[0:1, sl] += _colsum(df * (1.0 - sg))
            else:
                da_ref[:, sl] = dq * (HD ** -0.5)
                db_ref[:, sl] = dk
                dc_ref[:, sl] = dv
                dg_parts.append(dg)
        if not hg:
            dlogits = jnp.concatenate(dg_parts, axis=1) * (1.0 / GATE_NORM) * (1.0 - _sigmoid(logits))
            dlr_ref[...] = _dot_nt(dlogits, wgk_ref[...])
            dwgk_ref[...] += _dot_tn(lr, dlogits)
            dbias_ref[0:1, :] += _colsum(dlogits)

    def chunk_of(j):
        return _chunk_index(n_chunks - 1 - j, n_ctx_chunks, n_chunks, fw)

    def cmap(blk, width=HW):
        return pl.BlockSpec((CHUNK, width), lambda j: (chunk_of(j), blk))

    fixed = lambda j: (0, 0)
    st_spec = pl.BlockSpec((1, NH, HD, HD), lambda j: (chunk_of(j), 0, 0, 0))
    big = jax.ShapeDtypeStruct((rows, HW), F32)
    if hg:
        in_specs = [cmap(cols[0]), cmap(cols[1]), cmap(cols[2]), pl.BlockSpec((4, HW), fixed), st_spec, cmap(0)]
        args = (p, p, p, side, states, d_o)
        out_shape = [big, big, big, jax.ShapeDtypeStruct((8, HW), F32)]
        out_specs = [cmap(0), cmap(0), cmap(0), pl.BlockSpec((8, HW), fixed)]
    else:
        in_specs = [cmap(cols[0]), cmap(cols[1]), cmap(cols[2]), cmap(OFF_LR // 128, 128),
                    pl.BlockSpec((128, HW), fixed), pl.BlockSpec((1, HW), fixed), st_spec, cmap(0)]
        args = (p, p, p, p, side[0], side[1], states, d_o)
        out_shape = [big, big, big, jax.ShapeDtypeStruct((rows, 128), F32), jax.ShapeDtypeStruct((128, HW), F32),
                     jax.ShapeDtypeStruct((8, HW), F32)]
        out_specs = [cmap(0), cmap(0), cmap(0), cmap(0, 128), pl.BlockSpec((128, HW), fixed),
                     pl.BlockSpec((8, HW), fixed)]
    return pl.pallas_call(
        body, name=name, grid=(n_chunks,), out_shape=out_shape, in_specs=in_specs, out_specs=out_specs,
        scratch_shapes=[pltpu.VMEM((NH, HD, HD), F32)],
        compiler_params=_cparams(dimension_semantics=("arbitrary",)),
    )(*args)


SMALL_ROWS = 56
ROWS_MOD_X = (0, 1, 8, 16, 17, 18)
ROWS_MOD_C = (2, 3)
ROW_PRE1, ROW_POST1, ROW_ONORM, ROW_PRE2, ROW_POST2, ROW_LB, ROW_BGK, ROW_WGK = 4, 9, 10, 19, 20, 24, 32, 40


def _reduce_small(gathered, lb_full, name):
    _, _, d = gathered.shape

    def body(g_ref, lb_ref, sum_ref, dmod_ref, dbmod_ref, dlb_ref):
        total = g_ref[0]
        for b in range(1, N_DEV):
            total = total + g_ref[b]
        sum_ref[...] = total
        dmod_ref[...] = jnp.zeros_like(dmod_ref)
        for m in range(N_MOD):
            col = slice(m * d, (m + 1) * d)
            acc = jnp.zeros((1, d), F32)
            for b in range(N_DEV):
                row = g_ref[b, ROWS_MOD_X[m]:ROWS_MOD_X[m] + 1, :]
                dmod_ref[b:b + 1, col] = row
                acc = acc + row
            if m < 2:
                ctx_row = total[ROWS_MOD_C[m]:ROWS_MOD_C[m] + 1, :]
                dmod_ref[8:9, col] = ctx_row
                acc = acc + ctx_row
            dbmod_ref[:, col] = acc
        lbv = lb_ref[...]
        for dd in range(2):
            lb = _sigmoid(lbv[dd:dd + 1, :] - lbv[2 + dd:3 + dd, :])
            gl = total[ROW_LB:ROW_LB + 1, dd * HW:(dd + 1) * HW] * lb * (1.0 - lb)
            dlb_ref[dd:dd + 1, :] = gl
            dlb_ref[2 + dd:3 + dd, :] = -gl

    return pl.pallas_call(
        body, name=name,
        out_shape=[jax.ShapeDtypeStruct((SMALL_ROWS, d), F32), jax.ShapeDtypeStruct((16, N_MOD * d), F32),
                   jax.ShapeDtypeStruct((1, N_MOD * d), F32), jax.ShapeDtypeStruct((4, HW), F32)],
        in_specs=[VMEM_SPEC] * 2, out_specs=[VMEM_SPEC] * 4, compiler_params=_cparams(),
    )(gathered, lb_full)


def _c_ctx_grad(gathered, c_ctx_row, name):
    def body(g_ref, c_ref, o_ref):
        acc = g_ref[0, 0:1, :]
        for chip in range(1, N_CHIP):
            acc = acc + g_ref[2 * chip, 0:1, :]
        o_ref[...] = acc * _dsilu(c_ref[...])

    return pl.pallas_call(
        body, name=name, out_shape=jax.ShapeDtypeStruct(c_ctx_row.shape, F32),
        in_specs=[VMEM_SPEC] * 2, out_specs=VMEM_SPEC, compiler_params=_cparams(),
    )(gathered, c_ctx_row)


def _relayout_w_in(w):
    pad = jnp.zeros((w.shape[0], 128 - 2 * RANK), w.dtype)
    return jnp.concatenate([w[:, :9 * HW], w[:, 9 * HW + 2 * RANK:], w[:, 9 * HW:9 * HW + 2 * RANK], pad], axis=1)


def _unrelayout_w_in(g_main, g_lr):
    return jnp.concatenate([g_main[:, :9 * HW], g_lr[:, :2 * RANK], g_main[:, 9 * HW:]], axis=1)


def _blocked(full, n_blocks):
    k, n = full.shape
    return full.reshape(k, n_blocks, n // n_blocks).transpose(1, 0, 2)


def _unblocked(blocks):
    nb, k, n = blocks.shape
    return blocks.transpose(1, 0, 2).reshape(k, nb * n)


def _sample_front(x0, ctx0, modc, modx, norm_pre1, lb_full, gla_side, w_in_r):
    ctx_len = ctx0.shape[0]
    n_ctx_tiles = ctx_len // TM
    n_ctx_chunks = ctx_len // CHUNK
    z = jnp.concatenate([ctx0, x0], axis=0)
    h1, p = _in_projection(z, modc, modx, norm_pre1, w_in_r, n_ctx_tiles, "in_projection")
    o_hg_fw, st_hg_fw = _scan_fwd(p, lb_full, n_ctx_chunks, True, "hg", "scan_hg_fw")
    o_hg_bw, st_hg_bw = _scan_fwd(p, lb_full, n_ctx_chunks, False, "hg", "scan_hg_bw")
    o_gla_fw, st_gla_fw = _scan_fwd(p, gla_side[0], n_ctx_chunks, True, "gla", "scan_gla_fw")
    o_gla_bw, st_gla_bw = _scan_fwd(p, gla_side[1], n_ctx_chunks, False, "gla", "scan_gla_bw")
    return dict(z=z, h1=h1, p=p, o_list=[o_hg_fw, o_hg_bw, o_gla_fw, o_gla_bw],
                states=[st_hg_fw, st_hg_bw, st_gla_fw, st_gla_bw])


def _sample_back(front, x0, ctx0, target0, modc, modx, norm_pre1, norms, onorms, lb_full, gla_side, w_in_r, wbh, wbg,
                 wout, wg, wu, wd):
    seq, d = x0.shape
    ctx_len = ctx0.shape[0]
    n_ctx_tiles = ctx_len // TM
    n_tiles = (ctx_len + seq) // TM
    n_ctx_chunks = ctx_len // CHUNK
    z, h1, p, o_list = front["z"], front["h1"], front["p"], front["o_list"]
    st_hg_fw, st_hg_bw, st_gla_fw, st_gla_bw = front["states"]
    (loss_part, dz2, y1, merged, og_hg, og_gla, h2, a_act, du, dv, dy2, stat_ffn) = _mixer_ffn(
        x0, p, o_list, modx, norms, onorms, wbh, wbg, wout, wg, wu, wd, target0, n_ctx_tiles, "mixer_ffn")

    (d_ohg, d_ogla, d_hgate, d_ggate, d_ghg, d_ggla, dy1, db_hg, db_gla, stat_mix) = _mixer_tail_bwd(
        x0, p, o_list, dz2, y1, modx, norms, onorms, wbh, wbg, wout, n_ctx_tiles, n_tiles, "mixer_tail_bwd")
    dhq_f, dhi_f, dhf_f, dlb_f = _scan_bwd(p, lb_full, st_hg_fw, d_ohg, n_ctx_chunks, True, "hg", "scan_hg_fw_bwd")
    dhq_b, dhi_b, dhf_b, dlb_b = _scan_bwd(p, lb_full, st_hg_bw, d_ohg, n_ctx_chunks, False, "hg", "scan_hg_bw_bwd")
    dgq_f, dgk_f, dgv_f, dlr_f, dwgk_f, dbgk_f = _scan_bwd(p, gla_side[0], st_gla_fw, d_ogla, n_ctx_chunks, True, "gla",
                                                           "scan_gla_fw_bwd")
    dgq_b, dgk_b, dgv_b, dlr_b, dwgk_b, dbgk_b = _scan_bwd(p, gla_side[1], st_gla_bw, d_ogla, n_ctx_chunks, False, "gla",
                                                           "scan_gla_bw_bwd")
    pieces = [dhq_f, dhq_b, dhi_f, dhi_b, dhf_f, dhf_b, d_hgate, dgq_f, dgq_b, dgk_f, dgk_b, dgv_f, dgv_b, d_ggate,
              d_ghg, d_ggla, dlr_f, dlr_b]
    dp, grad_x, stat_in = _in_projection_bwd(z, dz2, modc, modx, norm_pre1, w_in_r, pieces, n_ctx_tiles,
                                             "in_projection_bwd")

    g_in_main = _weight_grad_cols(h1, dp, OFF_LR, "grad_w_in_main")
    g_in_lr = _weight_grad(h1, dp, "grad_w_in_lr", col_block=(128, OFF_LR // 128))
    dff = wg.shape[1]
    tn_ff = dff // N_CHIP if (dff // N_CHIP) % 128 == 0 else 256
    return dict(
        loss_part=loss_part, grad_x=grad_x, stat_in=stat_in, stat_mix=stat_mix, stat_ffn=stat_ffn,
        dlb=(dlb_f, dlb_b), dwgk=(dwgk_f, dwgk_b), dbgk=(dbgk_f, dbgk_b),
        g_w_in=_unrelayout_w_in(g_in_main, g_in_lr),
        g_br_hg=_weight_grad(og_hg, db_hg, "grad_w_br_hg"),
        g_br_gla=_weight_grad(og_gla, db_gla, "grad_w_br_gla"),
        g_out=_weight_grad(merged, dy1, "grad_w_out"),
        g_gate=_weight_grad(h2, du, "grad_w_ff_gate", tn=tn_ff),
        g_up=_weight_grad(h2, dv, "grad_w_ff_up", tn=tn_ff),
        g_down=_weight_grad(a_act, dy2, "grad_w_ff_down", tk=tn_ff))


def kernel(x, c, ctx, c_ctx, w_mod, b_mod, norm_pre1, norm_post1, norm_pre2, norm_post2, w_in, hg_lb, hg_onorm, gla_w_gk, gla_b_gk, gla_onorm, w_br_hg, w_br_gla, w_out, w_ff_gate, w_ff_up, w_ff_down, loss_target, m_c_ctx, m_w_mod, m_b_mod, m_norm_pre1, m_norm_post1, m_norm_pre2, m_norm_post2, m_w_in, m_hg_lb, m_hg_onorm, m_gla_w_gk, m_gla_b_gk, m_gla_onorm, m_w_br_hg, m_w_br_gla, m_w_out, m_w_ff_gate, m_w_ff_up, m_w_ff_down, v_c_ctx, v_w_mod, v_b_mod, v_norm_pre1, v_norm_post1, v_norm_pre2, v_norm_post2, v_w_in, v_hg_lb, v_hg_onorm, v_gla_w_gk, v_gla_b_gk, v_gla_onorm, v_w_br_hg, v_w_br_gla, v_w_out, v_w_ff_gate, v_w_ff_up, v_w_ff_down):
    seq, d = x.shape[1], x.shape[2]
    ctx_len = ctx.shape[1]
    assert seq % TM == 0 and ctx_len % TM == 0 and d == 2 * HW
    ax, ay, ac = lax.axis_index("x"), lax.axis_index("y"), lax.axis_index("c")
    chip = 2 * ax + ay
    dev = 2 * chip + ac
    c_arr = jnp.reshape(ac, (1,)).astype(jnp.int32)

    nc = d // 128
    pad8 = lambda a: jnp.pad(a, ((0, -a.shape[0] % 8), (0, 0)))
    small1 = jnp.concatenate([c.reshape(nc, 128), pad8(hg_lb.reshape(4, 128)), gla_w_gk.reshape(2 * RANK, 128),
                              pad8(gla_b_gk.reshape(2, 128))], axis=0)
    got1 = _allgather8(small1, "gather_small_params")
    c_all = got1[:, :nc, :].reshape(N_DEV, d)
    per_chip = got1[0::2]
    lb_full = per_chip[:, nc:nc + 4, :].transpose(1, 0, 2).reshape(4, HW)
    wgk_full = per_chip[:, nc + 8:nc + 8 + 2 * RANK, :].transpose(1, 0, 2).reshape(2, RANK, HW)
    bgk_full = per_chip[:, nc + 8 + 2 * RANK:nc + 10 + 2 * RANK, :].transpose(1, 0, 2).reshape(2, HW)
    wgk_pad = [jnp.zeros((128, HW), F32).at[dd * RANK:(dd + 1) * RANK].set(wgk_full[dd]) for dd in range(2)]
    bgk = [bgk_full[dd:dd + 1] for dd in range(2)]

    n_mod_cols = w_mod.shape[2]
    cond = jnp.concatenate([c_all, pad8(c_ctx.reshape(1, d))], axis=0)
    b_cols = lax.dynamic_slice(b_mod, (0, chip * n_mod_cols), (1, n_mod_cols))
    mod_part = _mod_forward(cond, w_mod[0], b_cols, "mod_forward")
    mod_got = _allgather8(mod_part, "gather_mod")
    mod_all = mod_got[0::2].transpose(1, 0, 2).reshape(16, N_CHIP * n_mod_cols)
    modx = pad8(lax.dynamic_slice(mod_all, (dev, 0), (1, N_MOD * d)).reshape(N_MOD, d))
    modc = pad8(mod_all[8].reshape(N_MOD, d))

    shards = [_cast_bf16(w_in[0], "cast_w_in"), _cast_bf16(w_br_hg[0], "cast_w_br_hg"),
              _cast_bf16(w_br_gla[0], "cast_w_br_gla"), _cast_bf16(w_out[0], "cast_w_out"),
              _cast_bf16(w_ff_gate[0], "cast_w_ff_gate"), _cast_bf16(w_ff_up[0], "cast_w_ff_up"),
              _cast_bf16(w_ff_down[0], "cast_w_ff_down")]
    gathered_in = _gather_shards(shards[:1], "gather_w_in")
    sems, rest_thru, lands, token = _gather_start(shards[1:], "gather_rest_start")
    w_in_r = _relayout_w_in(_unblocked(gathered_in[0]))

    norms = jnp.concatenate([norm_pre1, norm_post1, norm_pre2, norm_post2, jnp.zeros((4, d), F32)], axis=0)
    onorms = jnp.zeros((8, d), F32).at[0, :HD].set(hg_onorm[0]).at[1, :HD].set(gla_onorm[0])
    gla_side = [(wgk_pad[dd], bgk[dd]) for dd in range(2)]
    modx = modx + token[0, 0]
    front = _sample_front(x[0], ctx[0], modc, modx, norm_pre1, lb_full, gla_side, w_in_r)
    rest_thru, lands = _gather_wait(sems, rest_thru, lands, front["o_list"][3], "gather_rest_wait")
    gathered = _gather_finish(rest_thru, lands, "gather_rest_finish")
    wbh, wbg = _unblocked(gathered[0]), _unblocked(gathered[1])
    wout = gathered[2].reshape(d, d)
    wg, wu = _unblocked(gathered[3]), _unblocked(gathered[4])
    wd = gathered[5].reshape(wg.shape[1], d)
    r = _sample_back(front, x[0], ctx[0], loss_target[0], modc, modx, norm_pre1, norms, onorms, lb_full, gla_side,
                     w_in_r, wbh, wbg, wout, wg, wu, wd)
    loss_part, grad_x, stat_in, stat_mix, stat_ffn = (r[k] for k in ("loss_part", "grad_x", "stat_in", "stat_mix",
                                                                     "stat_ffn"))
    (dlb_f, dlb_b), (dwgk_f, dwgk_b), (dbgk_f, dbgk_b) = r["dlb"], r["dwgk"], r["dbgk"]
    dff = wg.shape[1]

    full = [_blocked(r["g_w_in"], N_CHIP), _blocked(r["g_br_hg"], N_CHIP), _blocked(r["g_br_gla"], N_CHIP),
            r["g_out"].reshape(N_CHIP, d // N_CHIP, d), _blocked(r["g_gate"], N_CHIP), _blocked(r["g_up"], N_CHIP),
            r["g_down"].reshape(N_CHIP, dff // N_CHIP, d)]
    from_sibling = _send_other_half(full, "grads_to_sibling")
    names = ["w_in", "w_br_hg", "w_br_gla", "w_out", "w_ff_gate", "w_ff_up", "w_ff_down"]
    pairs = [_pair_sum(c_arr, f, r, "pair_sum_" + nm) for f, r, nm in zip(full, from_sibling, names)]
    at_owner = _blocks_to_owner(pairs, "grads_to_owner")
    own_half = [_sum_chips(g, "chip_sum_" + nm) for g, nm in zip(at_owner, names)]
    other_half = _swap_with_sibling(own_half, "halves_to_sibling")
    big_w = [w_in, w_br_hg, w_br_gla, w_out, w_ff_gate, w_ff_up, w_ff_down]
    big_m = [m_w_in, m_w_br_hg, m_w_br_gla, m_w_out, m_w_ff_gate, m_w_ff_up, m_w_ff_down]
    big_v = [v_w_in, v_w_br_hg, v_w_br_gla, v_w_out, v_w_ff_gate, v_w_ff_up, v_w_ff_down]
    big = {}
    for nm, own, oth, w_, m_, v_ in zip(names, own_half, other_half, big_w, big_m, big_v):
        res = _adamw_halves(c_arr, own, oth, w_[0], m_[0], v_[0], "adamw_" + nm)
        big[nm] = [r[None] for r in res]

    small2 = jnp.concatenate([
        stat_in, stat_mix, stat_ffn, jnp.concatenate([dlb_f, dlb_b], axis=1), jnp.concatenate([dbgk_f, dbgk_b], axis=1),
        jnp.concatenate([dwgk_f[0:RANK], dwgk_b[RANK:2 * RANK]], axis=1)], axis=0)
    assert small2.shape[0] == SMALL_ROWS
    got2 = _allgather8(small2, "gather_small_grads")
    total, dmod_all, g_b_mod, g_lb_full = _reduce_small(got2, lb_full, "reduce_small")
    dmod_cols = lax.dynamic_slice(dmod_all, (0, chip * n_mod_cols), (16, n_mod_cols))
    g_w_mod, cctx_part = _mod_backward(cond, w_mod[0], dmod_cols, "mod_backward")
    got3 = _allgather8(cctx_part, "gather_c_ctx_grad")
    g_c_ctx = _c_ctx_grad(got3, c_ctx.reshape(1, d), "c_ctx_grad")

    g_pre1, g_post1, g_pre2, g_post2 = (total[r_:r_ + 1] for r_ in (ROW_PRE1, ROW_POST1, ROW_PRE2, ROW_POST2))
    g_hg_on, g_gla_on = total[ROW_ONORM:ROW_ONORM + 1, 0:HD], total[ROW_ONORM:ROW_ONORM + 1, HD:2 * HD]
    n_lb = hg_lb.shape[2]
    g_hg_lb = lax.dynamic_slice(g_lb_full, (0, chip * n_lb), (4, n_lb))
    g_bgk = lax.dynamic_slice(total[ROW_BGK:ROW_BGK + 1].reshape(2, HW), (0, chip * n_lb), (2, n_lb))
    g_wgk_full = total[ROW_WGK:ROW_WGK + RANK].reshape(RANK, 2, HW).transpose(1, 0, 2).reshape(2 * RANK, HW)
    g_wgk = lax.dynamic_slice(g_wgk_full, (0, chip * n_lb), (2 * RANK, n_lb))

    small_items = [
        (g_c_ctx, c_ctx.reshape(1, d), m_c_ctx.reshape(1, d), v_c_ctx.reshape(1, d)),
        (g_b_mod, b_mod, m_b_mod, v_b_mod),
        (g_pre1, norm_pre1, m_norm_pre1, v_norm_pre1),
        (g_post1, norm_post1, m_norm_post1, v_norm_post1),
        (g_pre2, norm_pre2, m_norm_pre2, v_norm_pre2),
        (g_post2, norm_post2, m_norm_post2, v_norm_post2),
        (g_hg_lb, hg_lb.reshape(4, n_lb), m_hg_lb.reshape(4, n_lb), v_hg_lb.reshape(4, n_lb)),
        (g_hg_on, hg_onorm, m_hg_onorm, v_hg_onorm),
        (g_wgk, gla_w_gk.reshape(2 * RANK, n_lb), m_gla_w_gk.reshape(2 * RANK, n_lb), v_gla_w_gk.reshape(2 * RANK, n_lb)),
        (g_bgk, gla_b_gk.reshape(2, n_lb), m_gla_b_gk.reshape(2, n_lb), v_gla_b_gk.reshape(2, n_lb)),
        (g_gla_on, gla_onorm, m_gla_onorm, v_gla_onorm),
    ]
    small_res = _adamw_whole(small_items, "adamw_small")
    mod_res = _adamw_tiled(g_w_mod, w_mod[0], m_w_mod[0], v_w_mod[0], "adamw_w_mod")

    loss = lax.psum(loss_part[0, 0], ("x", "y", "c"))

    shapes = dict(c_ctx=c_ctx.shape, b_mod=b_mod.shape, norm_pre1=norm_pre1.shape, norm_post1=norm_post1.shape,
                  norm_pre2=norm_pre2.shape, norm_post2=norm_post2.shape, hg_lb=hg_lb.shape, hg_onorm=hg_onorm.shape,
                  gla_w_gk=gla_w_gk.shape, gla_b_gk=gla_b_gk.shape, gla_onorm=gla_onorm.shape)
    small_names = ["c_ctx", "b_mod", "norm_pre1", "norm_post1", "norm_pre2", "norm_post2", "hg_lb", "hg_onorm",
                   "gla_w_gk", "gla_b_gk", "gla_onorm"]
    grads, deltas, new_m, new_v = {}, {}, {}, {}
    for nm, item, res in zip(small_names, small_items, small_res):
        grads[nm] = item[0].reshape(shapes[nm])
        deltas[nm], new_m[nm], new_v[nm] = (r.reshape(shapes[nm]) for r in res)
    grads["w_mod"] = g_w_mod[None]
    deltas["w_mod"], new_m["w_mod"], new_v["w_mod"] = (r[None] for r in mod_res)
    for nm in names:
        grads[nm], deltas[nm], new_m[nm], new_v[nm] = big[nm]
    order = ["c_ctx", "w_mod", "b_mod", "norm_pre1", "norm_post1", "norm_pre2", "norm_post2", "w_in", "hg_lb",
             "hg_onorm", "gla_w_gk", "gla_b_gk", "gla_onorm", "w_br_hg", "w_br_gla", "w_out", "w_ff_gate", "w_ff_up",
             "w_ff_down"]
    return (loss, grad_x[None], *[grads[n] for n in order], *[deltas[n] for n in order],
            *[new_m[n] for n in order], *[new_v[n] for n in order])


def _weight_grad_cols(xs, dy, n_cols, name, tk=256, tn=512):
    rows = dy.shape[0]
    k = xs.shape[1]

    def body(x_ref, dy_ref, o_ref):
        o_ref[...] = _dot_tn(x_ref[...], dy_ref[...])

    return pl.pallas_call(
        body, name=name, grid=(k // tk, n_cols // tn),
        out_shape=jax.ShapeDtypeStruct((k, n_cols), F32),
        in_specs=[pl.BlockSpec((rows, tk), lambda i, j: (0, i)), pl.BlockSpec((rows, tn), lambda i, j: (0, j))],
        out_specs=pl.BlockSpec((tk, tn), lambda i, j: (i, j)),
        compiler_params=_cparams(dimension_semantics=("parallel", "parallel")),
    )(xs, dy)
```

```python
import functools

import jax
import jax.numpy as jnp
from jax import lax
from jax.experimental import pallas as pl
from jax.experimental.pallas import tpu as pltpu

F32 = jnp.float32
BF16 = jnp.bfloat16
HIGHEST = lax.Precision.HIGHEST
MESH = pl.DeviceIdType.MESH

EPS = 1e-6
CHUNK = 64
SUB = 16
NSUB = CHUNK // SUB
NH = 4
HD = 128
HW = NH * HD
RANK = 16
GATE_NORM = 16.0
N_MOD = 6
TM = 256
TM_FFN = 128
N_DEV = 8
N_CHIP = 4
VMEM_LIMIT = 56 * 1024 * 1024

ADAM_LR = 0.001
ADAM_B1 = 0.9
ADAM_B2 = 0.999
ADAM_EPS = 1e-08
ADAM_WD = 0.01
ADAM_STEP = 10

VMEM_SPEC = pl.BlockSpec(memory_space=pltpu.VMEM)
ANY_SPEC = pl.BlockSpec(memory_space=pl.ANY)
HBM_SPEC = pl.BlockSpec(memory_space=pltpu.HBM)
SEM_SPEC = pl.BlockSpec(memory_space=pltpu.SEMAPHORE)
EFFECT = pltpu.SideEffectType.DATAFLOW_SIDE_EFFECTING


def _cparams(**kw):
    return pltpu.CompilerParams(vmem_limit_bytes=VMEM_LIMIT, **kw)


def _dot(a, b):
    return jnp.dot(a.astype(BF16), b.astype(BF16), preferred_element_type=F32)


def _dot_nt(a, b):
    return lax.dot_general(a.astype(BF16), b.astype(BF16), (((1,), (1,)), ((), ())), preferred_element_type=F32)


def _dot_tn(a, b):
    return lax.dot_general(a.astype(BF16), b.astype(BF16), (((0,), (0,)), ((), ())), preferred_element_type=F32)


def _sigmoid(x):
    return 1.0 / (1.0 + jnp.exp(-x))


def _silu(x):
    return x * _sigmoid(x)


def _dsilu(x):
    s = _sigmoid(x)
    return s * (1.0 + x * (1.0 - s))


def _log_sigmoid(x):
    return jnp.minimum(x, 0.0) - jnp.log(1.0 + jnp.exp(-jnp.abs(x)))


def _colsum(a):
    return jnp.sum(a, axis=0, keepdims=True)


def _rms(a):
    r = lax.rsqrt(jnp.mean(a * a, axis=-1, keepdims=True) + EPS)
    return a * r, r


def _rms_bwd(dn, n, r):
    return r * (dn - n * jnp.mean(dn * n, axis=-1, keepdims=True))


def _place():
    x, y, c = lax.axis_index("x"), lax.axis_index("y"), lax.axis_index("c")
    chips = [(1 - x, y), (x, 1 - y), (1 - x, 1 - y)]
    return x, y, c, chips


def _allgather8(v, name):
    rows, cols = v.shape

    def body(x_ref, out_ref, send_sems, recv_sems, local_sem):
        x, y, c, chips = _place()
        me, sibling = (x, y, c), (x, y, 1 - c)

        def blk(px, py, pc):
            return out_ref.at[4 * px + 2 * py + pc]

        def copy(k, block, to, src=None):
            return pltpu.make_async_remote_copy(
                src_ref=blk(*block) if src is None else src, dst_ref=blk(*block),
                send_sem=send_sems.at[k], recv_sem=recv_sems.at[k], device_id=to, device_id_type=MESH)

        mine = pltpu.make_async_copy(x_ref, blk(*me), local_sem)
        mine.start()
        first = [copy(0, me, sibling, src=x_ref)]
        first += [copy(1 + j, me, (*chip, c), src=x_ref) for j, chip in enumerate(chips)]
        for cp in first:
            cp.start()
        passed = [copy(4 + j, (*chip, c), sibling) for j, chip in enumerate(chips)]
        for j, chip in enumerate(chips):
            copy(1 + j, (*chip, c), me).wait_recv()
            passed[j].start()
        copy(0, sibling, me).wait_recv()
        for j, chip in enumerate(chips):
            copy(4 + j, (*chip, 1 - c), me).wait_recv()
        for cp in first + passed:
            cp.wait_send()
        mine.wait()

    return pl.pallas_call(
        body, name=name,
        out_shape=jax.ShapeDtypeStruct((N_DEV, rows, cols), v.dtype),
        in_specs=[VMEM_SPEC], out_specs=VMEM_SPEC,
        scratch_shapes=[pltpu.SemaphoreType.DMA((7,)), pltpu.SemaphoreType.DMA((7,)), pltpu.SemaphoreType.DMA],
    )(v)


def _cast_into_blocks(chip_arr, w, name):
    rows, cols = w.shape
    tr = _row_tile(rows, 16, 256)

    def body(chip_ref, w_ref, o_ref):
        o_ref[0] = w_ref[...].astype(BF16)

    return pl.pallas_call(
        body, name=name,
        grid_spec=pltpu.PrefetchScalarGridSpec(
            num_scalar_prefetch=1, grid=(rows // tr,),
            in_specs=[pl.BlockSpec((tr, cols), lambda i, chip_ref: (i, 0))],
            out_specs=pl.BlockSpec((1, tr, cols), lambda i, chip_ref: (chip_ref[0], i, 0))),
        out_shape=jax.ShapeDtypeStruct((N_CHIP, rows, cols), BF16),
        compiler_params=_cparams(dimension_semantics=("parallel",)),
    )(chip_arr, w)


def _half_rows(ref, chip_id, pc):
    h = ref.shape[1] // 2
    return ref.at[chip_id, pl.ds(pl.multiple_of(pc * h, 8), h), :]


def _gather_blocks(lands, name):
    n = len(lands)

    def body(*refs):
        outs = refs[n:2 * n]
        send_sems, recv_sems = refs[2 * n:]
        x, y, c, chips = _place()
        me_chip = 2 * x + y
        sibling = (x, y, 1 - c)

        def copy(k, j, chip_id, pc, to):
            return pltpu.make_async_remote_copy(
                src_ref=_half_rows(outs[k], chip_id, pc), dst_ref=_half_rows(outs[k], chip_id, pc),
                send_sem=send_sems.at[k, j], recv_sem=recv_sems.at[k, j], device_id=to, device_id_type=MESH)

        started = []
        for k in range(n):
            for j, chip in enumerate(chips):
                cp = copy(k, j, me_chip, c, (*chip, c))
                cp.start()
                started.append(cp)
        for k in range(n):
            for j, (px, py) in enumerate(chips):
                copy(k, j, 2 * px + py, c, sibling).wait_recv()
                cp = copy(k, 3 + j, 2 * px + py, c, sibling)
                cp.start()
                started.append(cp)
        for k in range(n):
            for j, (px, py) in enumerate(chips):
                copy(k, 3 + j, 2 * px + py, 1 - c, sibling).wait_recv()
        for cp in started:
            cp.wait_send()

    return pl.pallas_call(
        body, name=name,
        out_shape=[jax.ShapeDtypeStruct(l.shape, l.dtype) for l in lands],
        in_specs=[ANY_SPEC] * n, out_specs=[ANY_SPEC] * n,
        input_output_aliases={i: i for i in range(n)},
        scratch_shapes=[pltpu.SemaphoreType.DMA((n, 6)), pltpu.SemaphoreType.DMA((n, 6))],
    )(*lands)


def _hbm(a):
    return pltpu.with_memory_space_constraint(a, pltpu.HBM)


def _blocks_start(lands, name):
    n = len(lands)
    n_sem = 3 * n

    def body(*refs):
        lnd = refs[:n]
        send_sems, recv_sems = refs[n:n + n_sem], refs[n + n_sem:n + 2 * n_sem]
        token = refs[-1]
        x, y, c, chips = _place()
        me_chip = 2 * x + y
        for k in range(n):
            for j, chip in enumerate(chips):
                pltpu.make_async_remote_copy(
                    src_ref=_half_rows(lnd[k], me_chip, c), dst_ref=_half_rows(lnd[k], me_chip, c),
                    send_sem=send_sems[3 * k + j], recv_sem=recv_sems[3 * k + j],
                    device_id=(*chip, c), device_id_type=MESH).start()
        token[...] = jnp.zeros_like(token)

    out = pl.pallas_call(
        body, name=name,
        out_shape=(*[pltpu.SemaphoreType.DMA(())] * (2 * n_sem),
                   *[pltpu.HBM(l.shape, l.dtype) for l in lands],
                   jax.ShapeDtypeStruct((8, 128), F32)),
        in_specs=[HBM_SPEC] * n,
        out_specs=(*[SEM_SPEC] * (2 * n_sem), *[HBM_SPEC] * n, VMEM_SPEC),
        input_output_aliases={i: 2 * n_sem + i for i in range(n)},
        compiler_params=pltpu.CompilerParams(has_side_effects=EFFECT),
    )(*[_hbm(l) for l in lands])
    return list(out[:2 * n_sem]), list(out[2 * n_sem:2 * n_sem + n]), out[-1]


def _blocks_wait(sems, lands, after, name):
    n = len(lands)
    n_sem = 3 * n

    def body(*refs):
        lnd = refs[:n]
        s_sems, r_sems = refs[n:n + n_sem], refs[n + n_sem:n + 2 * n_sem]
        x, y, c, chips = _place()
        me_chip = 2 * x + y
        for k in range(n):
            for j, (px, py) in enumerate(chips):
                cp = pltpu.make_async_remote_copy(
                    src_ref=_half_rows(lnd[k], me_chip, c), dst_ref=_half_rows(lnd[k], 2 * px + py, c),
                    send_sem=s_sems[3 * k + j], recv_sem=r_sems[3 * k + j],
                    device_id=(px, py, c), device_id_type=MESH)
                cp.wait_send()
                cp.wait_recv()

    out = pl.pallas_call(
        body, name=name,
        out_shape=tuple(pltpu.HBM(l.shape, l.dtype) for l in lands),
        in_specs=[HBM_SPEC] * n + [SEM_SPEC] * (2 * n_sem) + [ANY_SPEC],
        out_specs=[HBM_SPEC] * n,
        input_output_aliases={i: i for i in range(n)},
        compiler_params=pltpu.CompilerParams(has_side_effects=EFFECT),
    )(*lands, *sems, after)
    return list(out)


def _blocks_finish(lands, name):
    n = len(lands)

    def body(*refs):
        lnd = refs[n:2 * n]
        send_sems, recv_sems = refs[2 * n:]
        x, y, c, chips = _place()
        sibling = (x, y, 1 - c)

        def copy(k, j, chip_id, pc):
            return pltpu.make_async_remote_copy(
                src_ref=_half_rows(lnd[k], chip_id, pc), dst_ref=_half_rows(lnd[k], chip_id, pc),
                send_sem=send_sems.at[k, j], recv_sem=recv_sems.at[k, j], device_id=sibling, device_id_type=MESH)

        started = []
        for k in range(n):
            for j, (px, py) in enumerate(chips):
                cp = copy(k, j, 2 * px + py, c)
                cp.start()
                started.append(cp)
        for k in range(n):
            for j, (px, py) in enumerate(chips):
                copy(k, j, 2 * px + py, 1 - c).wait_recv()
        for cp in started:
            cp.wait_send()

    out = pl.pallas_call(
        body, name=name,
        out_shape=[jax.ShapeDtypeStruct(l.shape, l.dtype) for l in lands],
        in_specs=[ANY_SPEC] * n, out_specs=[ANY_SPEC] * n,
        input_output_aliases={i: i for i in range(n)},
        scratch_shapes=[pltpu.SemaphoreType.DMA((n, 3)), pltpu.SemaphoreType.DMA((n, 3))],
    )(*lands)
    return list(out)


def _gather_start(shards, name):
    n = len(shards)
    n_sem = 3 * n

    def body(*refs):
        ins, lands = refs[:n], refs[n:2 * n]
        send_sems, recv_sems = refs[2 * n:2 * n + n_sem], refs[2 * n + n_sem:2 * n + 2 * n_sem]
        token = refs[-1]
        x, y, c, chips = _place()
        me_chip = 2 * x + y
        for k in range(n):
            h = shards[k].shape[0] // 2
            rows = pl.ds(pl.multiple_of(c * h, 8), h)
            for j, chip in enumerate(chips):
                pltpu.make_async_remote_copy(
                    src_ref=ins[k].at[rows, :], dst_ref=lands[k].at[me_chip, rows, :],
                    send_sem=send_sems[3 * k + j], recv_sem=recv_sems[3 * k + j],
                    device_id=(*chip, c), device_id_type=MESH).start()
        token[...] = jnp.zeros_like(token)

    lands = [_hbm(lax.empty((N_CHIP,) + s.shape, s.dtype)) for s in shards]
    out = pl.pallas_call(
        body, name=name,
        out_shape=(*[pltpu.SemaphoreType.DMA(())] * (2 * n_sem),
                   *[pltpu.HBM(s.shape, s.dtype) for s in shards],
                   *[pltpu.HBM(l.shape, l.dtype) for l in lands],
                   jax.ShapeDtypeStruct((8, 128), F32)),
        in_specs=[HBM_SPEC] * (2 * n),
        out_specs=(*[SEM_SPEC] * (2 * n_sem), *[HBM_SPEC] * (2 * n), VMEM_SPEC),
        input_output_aliases={i: 2 * n_sem + i for i in range(2 * n)},
        compiler_params=pltpu.CompilerParams(has_side_effects=EFFECT),
    )(*[_hbm(s) for s in shards], *lands)
    sems = list(out[:2 * n_sem])
    return sems, list(out[2 * n_sem:2 * n_sem + n]), list(out[2 * n_sem + n:2 * n_sem + 2 * n]), out[-1]


def _gather_wait(sems, shards, lands, after, name):
    n = len(shards)
    n_sem = 3 * n

    def body(*refs):
        ins, lnd = refs[:n], refs[n:2 * n]
        s_sems, r_sems = refs[2 * n:2 * n + n_sem], refs[2 * n + n_sem:2 * n + 2 * n_sem]
        x, y, c, chips = _place()
        for k in range(n):
            h = shards[k].shape[0] // 2
            rows = pl.ds(pl.multiple_of(c * h, 8), h)
            for j, (px, py) in enumerate(chips):
                cp = pltpu.make_async_remote_copy(
                    src_ref=ins[k].at[rows, :], dst_ref=lnd[k].at[2 * px + py, rows, :],
                    send_sem=s_sems[3 * k + j], recv_sem=r_sems[3 * k + j],
                    device_id=(px, py, c), device_id_type=MESH)
                cp.wait_send()
                cp.wait_recv()

    out = pl.pallas_call(
        body, name=name,
        out_shape=(*[pltpu.HBM(s.shape, s.dtype) for s in shards], *[pltpu.HBM(l.shape, l.dtype) for l in lands]),
        in_specs=[HBM_SPEC] * (2 * n) + [SEM_SPEC] * (2 * n_sem) + [ANY_SPEC],
        out_specs=[HBM_SPEC] * (2 * n),
        input_output_aliases={i: i for i in range(2 * n)},
        compiler_params=pltpu.CompilerParams(has_side_effects=EFFECT),
    )(*shards, *lands, *sems, after)
    return list(out[:n]), list(out[n:])


def _gather_finish(shards, lands, name):
    n = len(shards)

    def body(*refs):
        ins, lnd = refs[:n], refs[2 * n:3 * n]
        send_sems, recv_sems, local_sems = refs[3 * n:]
        x, y, c, chips = _place()
        me_chip = 2 * x + y
        sibling = (x, y, 1 - c)

        def half(k, chip_id, pc):
            h = shards[k].shape[0] // 2
            return lnd[k].at[chip_id, pl.ds(pl.multiple_of(pc * h, 8), h), :]

        def copy(k, j, chip_id, pc):
            return pltpu.make_async_remote_copy(
                src_ref=half(k, chip_id, pc), dst_ref=half(k, chip_id, pc),
                send_sem=send_sems.at[k, j], recv_sem=recv_sems.at[k, j], device_id=sibling, device_id_type=MESH)

        locals_, started = [], []
        for k in range(n):
            cp = pltpu.make_async_copy(ins[k], lnd[k].at[me_chip], local_sems.at[k])
            cp.start()
            locals_.append(cp)
            for j, (px, py) in enumerate(chips):
                cp = copy(k, j, 2 * px + py, c)
                cp.start()
                started.append(cp)
        for k in range(n):
            for j, (px, py) in enumerate(chips):
                copy(k, j, 2 * px + py, 1 - c).wait_recv()
        for cp in started:
            cp.wait_send()
        for cp in locals_:
            cp.wait()

    out = pl.pallas_call(
        body, name=name,
        out_shape=[jax.ShapeDtypeStruct(l.shape, l.dtype) for l in lands],
        in_specs=[ANY_SPEC] * (2 * n), out_specs=[ANY_SPEC] * n,
        input_output_aliases={n + i: i for i in range(n)},
        scratch_shapes=[pltpu.SemaphoreType.DMA((n, 3)), pltpu.SemaphoreType.DMA((n, 3)),
                        pltpu.SemaphoreType.DMA((n,))],
    )(*shards, *lands)
    return list(out)


def _send_other_half(arrs, name):
    n = len(arrs)

    def body(*refs):
        ins, outs = refs[:n], refs[n:2 * n]
        send_sems, recv_sems = refs[2 * n:]
        x, y, c, _ = _place()
        cps = []
        for k in range(n):
            h = arrs[k].shape[1] // 2
            cp = pltpu.make_async_remote_copy(
                src_ref=ins[k].at[:, pl.ds(pl.multiple_of((1 - c) * h, 8), h), :], dst_ref=outs[k],
                send_sem=send_sems.at[k], recv_sem=recv_sems.at[k], device_id=(x, y, 1 - c), device_id_type=MESH)
            cp.start()
            cps.append(cp)
        for cp in cps:
            cp.wait()

    return pl.pallas_call(
        body, name=name,
        out_shape=[jax.ShapeDtypeStruct((a.shape[0], a.shape[1] // 2, a.shape[2]), a.dtype) for a in arrs],
        in_specs=[ANY_SPEC] * n, out_specs=[ANY_SPEC] * n,
        scratch_shapes=[pltpu.SemaphoreType.DMA((n,)), pltpu.SemaphoreType.DMA((n,))],
    )(*arrs)


def _blocks_to_owner(arrs, name):
    n = len(arrs)

    def body(*refs):
        ins, outs = refs[:n], refs[n:2 * n]
        send_sems, recv_sems, local_sems = refs[2 * n:]
        x, y, c, chips = _place()
        me_chip = 2 * x + y
        locals_, started = [], []
        for k in range(n):
            cp = pltpu.make_async_copy(ins[k].at[me_chip], outs[k].at[me_chip], local_sems.at[k])
            cp.start()
            locals_.append(cp)

        def copy(k, j, src_block, dst_slot, to):
            return pltpu.make_async_remote_copy(
                src_ref=ins[k].at[src_block], dst_ref=outs[k].at[dst_slot],
                send_sem=send_sems.at[k, j], recv_sem=recv_sems.at[k, j], device_id=to, device_id_type=MESH)

        for k in range(n):
            for j, (px, py) in enumerate(chips):
                cp = copy(k, j, 2 * px + py, me_chip, (px, py, c))
                cp.start()
                started.append(cp)
        for k in range(n):
            for j, (px, py) in enumerate(chips):
                copy(k, j, me_chip, 2 * px + py, (px, py, c)).wait_recv()
        for cp in started:
            cp.wait_send()
        for cp in locals_:
            cp.wait()

    return pl.pallas_call(
        body, name=name,
        out_shape=[jax.ShapeDtypeStruct(a.shape, a.dtype) for a in arrs],
        in_specs=[ANY_SPEC] * n, out_specs=[ANY_SPEC] * n,
        scratch_shapes=[pltpu.SemaphoreType.DMA((n, 3)), pltpu.SemaphoreType.DMA((n, 3)),
                        pltpu.SemaphoreType.DMA((n,))],
    )(*arrs)


def _scatter_blocks(arrs, name):
    n = len(arrs)

    def body(*refs):
        ins, outs = refs[:n], refs[n:2 * n]
        send_sems, recv_sems = refs[2 * n:]
        x, y, c, chips = _place()
        me_chip = 2 * x + y

        def copy(k, j, src_block, dst_slot, to):
            return pltpu.make_async_remote_copy(
                src_ref=ins[k].at[src_block], dst_ref=outs[k].at[dst_slot],
                send_sem=send_sems.at[k, j], recv_sem=recv_sems.at[k, j], device_id=to, device_id_type=MESH)

        started = []
        for k in range(n):
            for j, (px, py) in enumerate(chips):
                cp = copy(k, j, 2 * px + py, me_chip, (px, py, c))
                cp.start()
                started.append(cp)
        for k in range(n):
            for j, (px, py) in enumerate(chips):
                copy(k, j, me_chip, 2 * px + py, (px, py, c)).wait_recv()
        for cp in started:
            cp.wait_send()

    return pl.pallas_call(
        body, name=name,
        out_shape=[jax.ShapeDtypeStruct(a.shape, a.dtype) for a in arrs],
        in_specs=[ANY_SPEC] * n, out_specs=[ANY_SPEC] * n,
        scratch_shapes=[pltpu.SemaphoreType.DMA((n, 3)), pltpu.SemaphoreType.DMA((n, 3))],
    )(*arrs)


def _sum_owner(chip_arr, pairs, got, name):
    nb, h, cols = got.shape
    tr = _row_tile(h, 16, 256)

    def body(chip_ref, own_ref, a_ref, b_ref, c_ref, o_ref):
        o_ref[...] = ((own_ref[0].astype(F32) + a_ref[0].astype(F32)) + b_ref[0].astype(F32)) + c_ref[0].astype(F32)

    def slot(off):
        return pl.BlockSpec((1, tr, cols), lambda i, chip_ref: ((chip_ref[0] + off) % N_CHIP, i, 0))

    return pl.pallas_call(
        body, name=name,
        grid_spec=pltpu.PrefetchScalarGridSpec(
            num_scalar_prefetch=1, grid=(h // tr,),
            in_specs=[slot(0), slot(1), slot(2), slot(3)],
            out_specs=pl.BlockSpec((tr, cols), lambda i, chip_ref: (i, 0))),
        out_shape=jax.ShapeDtypeStruct((h, cols), F32),
        compiler_params=_cparams(dimension_semantics=("parallel",)),
    )(chip_arr, pairs, got, got, got)


def _swap_with_sibling(arrs, name):
    n = len(arrs)

    def body(*refs):
        ins, outs = refs[:n], refs[n:2 * n]
        send_sems, recv_sems = refs[2 * n:]
        x, y, c, _ = _place()
        cps = []
        for k in range(n):
            cp = pltpu.make_async_remote_copy(
                src_ref=ins[k], dst_ref=outs[k], send_sem=send_sems.at[k], recv_sem=recv_sems.at[k],
                device_id=(x, y, 1 - c), device_id_type=MESH)
            cp.start()
            cps.append(cp)
        for cp in cps:
            cp.wait()

    return pl.pallas_call(
        body, name=name,
        out_shape=[jax.ShapeDtypeStruct(a.shape, a.dtype) for a in arrs],
        in_specs=[ANY_SPEC] * n, out_specs=[ANY_SPEC] * n,
        scratch_shapes=[pltpu.SemaphoreType.DMA((n,)), pltpu.SemaphoreType.DMA((n,))],
    )(*arrs)


def _row_tile(h, mult=8, cap=128):
    for t in range(cap - cap % mult, mult - 1, -mult):
        if h % t == 0:
            return t
    raise ValueError(h)


def _cast_bf16(a, name):
    rows, cols = a.shape
    tr = _row_tile(rows, 16, 256)

    def body(a_ref, o_ref):
        o_ref[...] = a_ref[...].astype(BF16)

    return pl.pallas_call(
        body, name=name, grid=(rows // tr,),
        out_shape=jax.ShapeDtypeStruct(a.shape, BF16),
        in_specs=[pl.BlockSpec((tr, cols), lambda i: (i, 0))],
        out_specs=pl.BlockSpec((tr, cols), lambda i: (i, 0)),
        compiler_params=_cparams(dimension_semantics=("parallel",)),
    )(a)


def _pair_sum(c_arr, full, recv, name):
    nb, rows, cols = full.shape
    h = rows // 2
    tr = _row_tile(h, 16, 256)
    steps = h // tr

    def body(c_ref, f_ref, r_ref, o_ref):
        o_ref[...] = (f_ref[...] + r_ref[...]).astype(BF16)

    return pl.pallas_call(
        body, name=name,
        grid_spec=pltpu.PrefetchScalarGridSpec(
            num_scalar_prefetch=1, grid=(nb, steps),
            in_specs=[pl.BlockSpec((1, tr, cols), lambda b, i, c_ref: (b, c_ref[0] * steps + i, 0)),
                      pl.BlockSpec((1, tr, cols), lambda b, i, c_ref: (b, i, 0))],
            out_specs=pl.BlockSpec((1, tr, cols), lambda b, i, c_ref: (b, i, 0))),
        out_shape=jax.ShapeDtypeStruct((nb, h, cols), BF16),
        compiler_params=_cparams(dimension_semantics=("parallel", "parallel")),
    )(c_arr, full, recv)


def _sum_chips(got, name):
    nb, h, cols = got.shape
    tr = _row_tile(h, 16, 256)

    def body(g_ref, o_ref):
        g = g_ref[...].astype(F32)
        o_ref[...] = ((g[0] + g[1]) + g[2]) + g[3]

    return pl.pallas_call(
        body, name=name, grid=(h // tr,),
        out_shape=jax.ShapeDtypeStruct((h, cols), F32),
        in_specs=[pl.BlockSpec((nb, tr, cols), lambda i: (0, i, 0))],
        out_specs=pl.BlockSpec((tr, cols), lambda i: (i, 0)),
        compiler_params=_cparams(dimension_semantics=("parallel",)),
    )(got)


def _adam_math(g, w, m, v):
    m1 = ADAM_B1 * m + (1.0 - ADAM_B1) * g
    v1 = ADAM_B2 * v + (1.0 - ADAM_B2) * (g * g)
    m_hat = m1 / (1.0 - ADAM_B1 ** ADAM_STEP)
    v_hat = v1 / (1.0 - ADAM_B2 ** ADAM_STEP)
    delta = -ADAM_LR * (m_hat / (jnp.sqrt(v_hat) + ADAM_EPS) + ADAM_WD * w)
    return delta, m1, v1


def _adamw_halves(c_arr, own, other, w, m, v, name):
    rows, cols = w.shape
    h = rows // 2
    tr = _row_tile(h)
    steps = h // tr

    def body(c_ref, own_ref, oth_ref, w_ref, m_ref, v_ref, g_out, d_out, m_out, v_out):
        g = jnp.where(pl.program_id(0) == c_ref[0], own_ref[...], oth_ref[...])
        d, m1, v1 = _adam_math(g, w_ref[...], m_ref[...], v_ref[...])
        g_out[...] = g
        d_out[...] = d
        m_out[...] = m1
        v_out[...] = v1

    half_spec = pl.BlockSpec((tr, cols), lambda p, i, c_ref: (i, 0))
    full_spec = pl.BlockSpec((tr, cols), lambda p, i, c_ref: (p * steps + i, 0))
    return pl.pallas_call(
        body, name=name,
        grid_spec=pltpu.PrefetchScalarGridSpec(
            num_scalar_prefetch=1, grid=(2, steps),
            in_specs=[half_spec, half_spec, full_spec, full_spec, full_spec],
            out_specs=[full_spec] * 4),
        out_shape=[jax.ShapeDtypeStruct(w.shape, F32)] * 4,
        compiler_params=_cparams(dimension_semantics=("parallel", "parallel")),
    )(c_arr, own, other, w, m, v)


def _adamw_whole(items, name):
    n = len(items)

    def body(*refs):
        ins, outs = refs[:4 * n], refs[4 * n:]
        for k in range(n):
            g, w, m, v = (r[...] for r in ins[4 * k:4 * k + 4])
            d, m1, v1 = _adam_math(g, w, m, v)
            outs[3 * k][...] = d
            outs[3 * k + 1][...] = m1
            outs[3 * k + 2][...] = v1

    flat = [a for it in items for a in it]
    shapes = [jax.ShapeDtypeStruct(it[1].shape, F32) for it in items for _ in range(3)]
    out = pl.pallas_call(
        body, name=name, out_shape=shapes,
        in_specs=[VMEM_SPEC] * (4 * n), out_specs=[VMEM_SPEC] * (3 * n),
        compiler_params=_cparams(),
    )(*flat)
    return [tuple(out[3 * k:3 * k + 3]) for k in range(n)]


def _adamw_tiled(g, w, m, v, name):
    rows, cols = w.shape
    tr = _row_tile(rows)

    def body(g_ref, w_ref, m_ref, v_ref, d_out, m_out, v_out):
        d, m1, v1 = _adam_math(g_ref[...], w_ref[...], m_ref[...], v_ref[...])
        d_out[...] = d
        m_out[...] = m1
        v_out[...] = v1

    spec = pl.BlockSpec((tr, cols), lambda i: (i, 0))
    return pl.pallas_call(
        body, name=name, grid=(rows // tr,),
        out_shape=[jax.ShapeDtypeStruct(w.shape, F32)] * 3,
        in_specs=[spec] * 4, out_specs=[spec] * 3,
        compiler_params=_cparams(dimension_semantics=("parallel",)),
    )(g, w, m, v)


def _mod_forward(cond, w_mod, b_mod_cols, name):
    def body(c_ref, w_ref, b_ref, o_ref):
        o_ref[...] = _dot(_silu(c_ref[...]), w_ref[...]) + b_ref[...]

    return pl.pallas_call(
        body, name=name, out_shape=jax.ShapeDtypeStruct((cond.shape[0], w_mod.shape[1]), F32),
        in_specs=[VMEM_SPEC] * 3, out_specs=VMEM_SPEC, compiler_params=_cparams(),
    )(cond, w_mod, b_mod_cols)


def _mod_backward(cond, w_mod, dmod_cols, name):
    def body(c_ref, w_ref, d_ref, gw_ref, gc_ref):
        s = _silu(c_ref[...])
        d = d_ref[...]
        gw_ref[...] = _dot_tn(s, d)
        gc_ref[...] = _dot_nt(d[8:16, :], w_ref[...])

    return pl.pallas_call(
        body, name=name,
        out_shape=[jax.ShapeDtypeStruct(w_mod.shape, F32), jax.ShapeDtypeStruct((8, w_mod.shape[0]), F32)],
        in_specs=[VMEM_SPEC] * 3, out_specs=[VMEM_SPEC] * 2, compiler_params=_cparams(),
    )(cond, w_mod, dmod_cols)


def _col_chunks(width, step=512):
    return [(s, min(step, width - s)) for s in range(0, width, step)]


def _in_projection(z, modc, modx, pre1, w_r, n_ctx_tiles, name):
    rows, d = z.shape
    width = w_r.shape[1]

    def body(z_ref, modc_ref, modx_ref, pre_ref, w_ref, h_ref, p_ref):
        is_ctx = pl.program_id(0) < n_ctx_tiles
        n, _ = _rms(z_ref[...])
        shift = jnp.where(is_ctx, modc_ref[0:1, :], modx_ref[0:1, :])
        scale = jnp.where(is_ctx, modc_ref[1:2, :], modx_ref[1:2, :])
        h = (n * pre_ref[...] * (1.0 + scale) + shift).astype(BF16)
        h_ref[...] = h
        for s, w in _col_chunks(width):
            p_ref[:, s:s + w] = jnp.dot(h, w_ref[:, s:s + w], preferred_element_type=F32)

    row = lambda i: (i, 0)
    fixed = lambda i: (0, 0)
    return pl.pallas_call(
        body, name=name, grid=(rows // TM,),
        out_shape=[jax.ShapeDtypeStruct((rows, d), BF16), jax.ShapeDtypeStruct((rows, width), F32)],
        in_specs=[pl.BlockSpec((TM, d), row), pl.BlockSpec((8, d), fixed), pl.BlockSpec((8, d), fixed),
                  pl.BlockSpec((1, d), fixed), VMEM_SPEC],
        out_specs=[pl.BlockSpec((TM, d), row), pl.BlockSpec((TM, width), row)],
        compiler_params=_cparams(dimension_semantics=("parallel",)),
    )(z, modc, modx, pre1, w_r)


C_HQ, C_HI, C_HF_FW, C_HF_BW, C_HGATE, C_GQ, C_GK, C_GV, C_GGATE = range(9)
OFF_GATE_HG = 9 * HW
OFF_LR = 13 * HW
P_WIDTH = OFF_LR + 128


def _head_norm_fwd(o, w):
    outs, ns, rs = [], [], []
    for h in range(NH):
        n, r = _rms(o[:, h * HD:(h + 1) * HD])
        ns.append(n)
        rs.append(r)
        outs.append(n * w)
    return jnp.concatenate(outs, axis=1), ns, rs


def _mixer_tail(z, o_hg, o_gla, p_hgate, p_ggate, p_gate_hg, p_gate_gla, hg_on, gla_on, wbh, wbg, wout):
    on_hg, n_hg, r_hg = _head_norm_fwd(o_hg, hg_on)
    on_gla, n_gla, r_gla = _head_norm_fwd(o_gla, gla_on)
    og_hg = (on_hg * _silu(p_hgate)).astype(BF16)
    og_gla = (on_gla * _silu(p_ggate)).astype(BF16)
    b_hg = jnp.dot(og_hg, wbh, preferred_element_type=F32)
    b_gla = jnp.dot(og_gla, wbg, preferred_element_type=F32)
    s_hg = _sigmoid(p_gate_hg)
    s_gla = _sigmoid(p_gate_gla)
    merged = (s_hg * b_hg + s_gla * b_gla).astype(BF16)
    y1 = jnp.dot(merged, wout, preferred_element_type=F32)
    return dict(on_hg=on_hg, n_hg=n_hg, r_hg=r_hg, on_gla=on_gla, n_gla=n_gla, r_gla=r_gla, og_hg=og_hg,
                og_gla=og_gla, b_hg=b_hg, b_gla=b_gla, s_hg=s_hg, s_gla=s_gla, merged=merged, y1=y1)


def _mixer_ffn(x_lat, p, o_list, modx, norms, onorms, w_br_hg, w_br_gla, w_out, w_gate, w_up, w_down, target,
               n_ctx_tiles, name):
    rows, d = x_lat.shape
    dff = w_gate.shape[1]
    inv_d = 1.0 / d

    def body(x_ref, ofw_hg, obw_hg, ofw_gla, obw_gla, p_hgate, p_ggate, p_ghg_a, p_ghg_b, p_ggla_a, p_ggla_b,
             modx_ref, norm_ref, on_ref, wbh_ref, wbg_ref, wout_ref, wg_ref, wu_ref, wd_ref, t_ref,
             loss_ref, dz2_ref, y1_ref, mrg_ref, oghg_ref, oggla_ref, h2_ref, a_ref, du_ref, dv_ref, dy2_ref,
             stat_ref):
        i = pl.program_id(0)
        post1, pre2, post2 = norm_ref[1:2, :], norm_ref[2:3, :], norm_ref[3:4, :]
        gate1, shift2, scale2, gate2 = modx_ref[2:3, :], modx_ref[3:4, :], modx_ref[4:5, :], modx_ref[5:6, :]
        p_gate_hg = jnp.concatenate([p_ghg_a[...], p_ghg_b[...]], axis=1)
        p_gate_gla = jnp.concatenate([p_ggla_a[...], p_ggla_b[...]], axis=1)
        t = _mixer_tail(x_ref[...], ofw_hg[...] + obw_hg[...], ofw_gla[...] + obw_gla[...], p_hgate[...],
                        p_ggate[...], p_gate_hg, p_gate_gla, on_ref[0:1, 0:HD], on_ref[1:2, 0:HD],
                        wbh_ref[...], wbg_ref[...], wout_ref[...])
        y1_ref[...] = t["y1"]
        mrg_ref[...] = t["merged"]
        oghg_ref[...] = t["og_hg"]
        oggla_ref[...] = t["og_gla"]
        n1, _ = _rms(t["y1"])
        z2 = x_ref[...] + n1 * post1 * gate1
        n2, r2 = _rms(z2)
        nw2 = n2 * pre2
        h2 = (nw2 * (1.0 + scale2) + shift2).astype(BF16)
        h2_ref[...] = h2
        u = jnp.dot(h2, wg_ref[...], preferred_element_type=F32)
        v = jnp.dot(h2, wu_ref[...], preferred_element_type=F32)
        su = _silu(u)
        a = (su * v).astype(BF16)
        a_ref[...] = a
        y2 = jnp.dot(a, wd_ref[...], preferred_element_type=F32)
        n3, r3 = _rms(y2)
        z3 = z2 + n3 * post2 * gate2
        err = z3 - t_ref[...]
        part = 0.5 * inv_d * jnp.sum(err * err)
        dz3 = err * inv_d
        dgate2 = _colsum(dz3 * n3 * post2)
        tt = dz3 * gate2
        dpost2 = _colsum(tt * n3)
        dy2 = _rms_bwd(tt * post2, n3, r3).astype(BF16)
        dy2_ref[...] = dy2
        da = _dot_nt(dy2, wd_ref[...])
        du = (da * v * _dsilu(u)).astype(BF16)
        dv = (da * su).astype(BF16)
        du_ref[...] = du
        dv_ref[...] = dv
        dh2 = _dot_nt(du, wg_ref[...]) + _dot_nt(dv, wu_ref[...])
        dshift2 = _colsum(dh2)
        dscale2 = _colsum(dh2 * nw2)
        dnw2 = dh2 * (1.0 + scale2)
        dpre2 = _colsum(dnw2 * n2)
        dz2_ref[...] = dz3 + _rms_bwd(dnw2 * pre2, n2, r2)

        @pl.when(i == 0)
        def _():
            stat_ref[...] = jnp.zeros_like(stat_ref)
            loss_ref[...] = jnp.zeros_like(loss_ref)

        for r, val in enumerate((dshift2, dscale2, dgate2, dpre2, dpost2)):
            stat_ref[r:r + 1, :] += val
        loss_ref[...] += part

    tm = TM_FFN
    ctx_tiles = n_ctx_tiles * (TM // tm)
    lat = lambda i: (i, 0)
    full = lambda i: (i + ctx_tiles, 0)
    fixed = lambda i: (0, 0)

    def pcol(blk):
        return pl.BlockSpec((tm, HW), lambda i: (i + ctx_tiles, blk))

    in_specs = ([pl.BlockSpec((tm, d), lat)] + [pl.BlockSpec((tm, HW), full)] * 4
                + [pcol(C_HGATE), pcol(C_GGATE), pcol(9), pcol(10), pcol(11), pcol(12)]
                + [pl.BlockSpec((8, d), fixed), pl.BlockSpec((8, d), fixed), pl.BlockSpec((8, d), fixed)]
                + [VMEM_SPEC] * 6 + [pl.BlockSpec((tm, d), lat)])
    bf = lambda w: jax.ShapeDtypeStruct((rows, w), BF16)
    out_shape = [jax.ShapeDtypeStruct((8, 128), F32), jax.ShapeDtypeStruct((rows, d), F32),
                 jax.ShapeDtypeStruct((rows, d), F32), bf(d), bf(HW), bf(HW), bf(d), bf(dff), bf(dff), bf(dff), bf(d),
                 jax.ShapeDtypeStruct((8, d), F32)]
    out_specs = [pl.BlockSpec((8, 128), fixed), pl.BlockSpec((tm, d), lat), pl.BlockSpec((tm, d), lat),
                 pl.BlockSpec((tm, d), lat), pl.BlockSpec((tm, HW), lat), pl.BlockSpec((tm, HW), lat),
                 pl.BlockSpec((tm, d), lat), pl.BlockSpec((tm, dff), lat), pl.BlockSpec((tm, dff), lat),
                 pl.BlockSpec((tm, dff), lat), pl.BlockSpec((tm, d), lat), pl.BlockSpec((8, d), fixed)]
    return pl.pallas_call(
        body, name=name, grid=(rows // tm,), out_shape=out_shape, in_specs=in_specs, out_specs=out_specs,
        compiler_params=_cparams(dimension_semantics=("arbitrary",)),
    )(x_lat, *o_list, p, p, p, p, p, p, modx, norms, onorms, w_br_hg, w_br_gla, w_out, w_gate, w_up, w_down, target)


def _mixer_tail_bwd(x_lat, p, o_list, dz2, y1, modx, norms, onorms, w_br_hg, w_br_gla, w_out, n_ctx_tiles, n_tiles,
                    name):
    rows, d = x_lat.shape
    total = n_tiles * TM

    def body(x_ref, ofw_hg, obw_hg, ofw_gla, obw_gla, p_hgate, p_ggate, p_ghg_a, p_ghg_b, p_ggla_a, p_ggla_b,
             dz2_ref, y1_ref, modx_ref, norm_ref, on_ref, wbh_ref, wbg_ref, wout_ref,
             dohg_ref, dogla_ref, dhgate_ref, dggate_ref, dghg_ref, dggla_ref, dy1_ref, dbhg_ref, dbgla_ref,
             stat_ref):
        i = pl.program_id(0)

        @pl.when(i == 0)
        def _():
            stat_ref[...] = jnp.zeros_like(stat_ref)

        @pl.when(i < n_ctx_tiles)
        def _():
            for ref in (dohg_ref, dogla_ref, dhgate_ref, dggate_ref, dghg_ref, dggla_ref):
                ref[...] = jnp.zeros_like(ref)

        @pl.when(i >= n_ctx_tiles)
        def _():
            post1, gate1 = norm_ref[1:2, :], modx_ref[2:3, :]
            hg_on, gla_on = on_ref[0:1, 0:HD], on_ref[1:2, 0:HD]
            p_gate_hg = jnp.concatenate([p_ghg_a[...], p_ghg_b[...]], axis=1)
            p_gate_gla = jnp.concatenate([p_ggla_a[...], p_ggla_b[...]], axis=1)
            ph, pg = p_hgate[...], p_ggate[...]
            t = _mixer_tail(x_ref[...], ofw_hg[...] + obw_hg[...], ofw_gla[...] + obw_gla[...], ph, pg,
                            p_gate_hg, p_gate_gla, hg_on, gla_on, wbh_ref[...], wbg_ref[...], wout_ref[...])
            dz2 = dz2_ref[...]
            n1, r1 = _rms(y1_ref[...])
            dgate1 = _colsum(dz2 * n1 * post1)
            tt = dz2 * gate1
            dpost1 = _colsum(tt * n1)
            dy1 = _rms_bwd(tt * post1, n1, r1).astype(BF16)
            dy1_ref[...] = dy1
            dmerged = _dot_nt(dy1, wout_ref[...])
            dghg_ref[...] = dmerged * t["b_hg"] * t["s_hg"] * (1.0 - t["s_hg"])
            dggla_ref[...] = dmerged * t["b_gla"] * t["s_gla"] * (1.0 - t["s_gla"])
            db_hg = (dmerged * t["s_hg"]).astype(BF16)
            db_gla = (dmerged * t["s_gla"]).astype(BF16)
            dbhg_ref[...] = db_hg
            dbgla_ref[...] = db_gla
            don_acc = []
            for (db, wb, pgate, on, ns, rs, gain, gate_ref, do_ref) in (
                    (db_hg, wbh_ref, ph, t["on_hg"], t["n_hg"], t["r_hg"], hg_on, dhgate_ref, dohg_ref),
                    (db_gla, wbg_ref, pg, t["on_gla"], t["n_gla"], t["r_gla"], gla_on, dggate_ref, dogla_ref)):
                dog = _dot_nt(db, wb[...])
                gate_ref[...] = dog * on * _dsilu(pgate)
                don = dog * _silu(pgate)
                acc = jnp.zeros((1, HD), F32)
                for h in range(NH):
                    sl = slice(h * HD, (h + 1) * HD)
                    acc = acc + _colsum(don[:, sl] * ns[h])
                    do_ref[:, sl] = _rms_bwd(don[:, sl] * gain, ns[h], rs[h])
                don_acc.append(acc)
            stat_ref[0:1, :] += dgate1
            stat_ref[1:2, :] += dpost1
            stat_ref[2:3, 0:HD] += don_acc[0]
            stat_ref[2:3, HD:2 * HD] += don_acc[1]

    lat = lambda i: (jnp.maximum(i - n_ctx_tiles, 0), 0)
    full = lambda i: (i, 0)
    fixed = lambda i: (0, 0)

    def pcol(blk):
        return pl.BlockSpec((TM, HW), lambda i: (i, blk))

    in_specs = ([pl.BlockSpec((TM, d), lat)] + [pl.BlockSpec((TM, HW), full)] * 4
                + [pcol(C_HGATE), pcol(C_GGATE), pcol(9), pcol(10), pcol(11), pcol(12)]
                + [pl.BlockSpec((TM, d), lat), pl.BlockSpec((TM, d), lat)]
                + [pl.BlockSpec((8, d), fixed)] * 3 + [VMEM_SPEC] * 3)
    f = lambda w: jax.ShapeDtypeStruct((total, w), F32)
    out_shape = [f(HW), f(HW), f(HW), f(HW), f(d), f(d), jax.ShapeDtypeStruct((rows, d), BF16),
                 jax.ShapeDtypeStruct((rows, d), BF16), jax.ShapeDtypeStruct((rows, d), BF16),
                 jax.ShapeDtypeStruct((8, d), F32)]
    out_specs = ([pl.BlockSpec((TM, HW), full)] * 4 + [pl.BlockSpec((TM, d), full)] * 2
                 + [pl.BlockSpec((TM, d), lat)] * 3 + [pl.BlockSpec((8, d), fixed)])
    return pl.pallas_call(
        body, name=name, grid=(n_tiles,), out_shape=out_shape, in_specs=in_specs, out_specs=out_specs,
        compiler_params=_cparams(dimension_semantics=("arbitrary",)),
    )(x_lat, *o_list, p, p, p, p, p, p, dz2, y1, modx, norms, onorms, w_br_hg, w_br_gla, w_out)


def _in_projection_bwd(z, dz2, modc, modx, pre1, w_r, pieces, n_ctx_tiles, name):
    rows, d = z.shape
    lat_rows = dz2.shape[0]
    width = w_r.shape[1]
    n_pieces = len(pieces)

    def body(*refs):
        z_ref, dz2_ref, modc_ref, modx_ref, pre_ref, w_ref = refs[:6]
        (dhq_f, dhq_b, dhi_f, dhi_b, dhf_f, dhf_b, dhgate, dgq_f, dgq_b, dgk_f, dgk_b, dgv_f, dgv_b, dggate,
         dghg, dggla, dlr_f, dlr_b) = refs[6:6 + n_pieces]
        dp_ref, gx_ref, stat_ref = refs[6 + n_pieces:]
        i = pl.program_id(0)
        is_ctx = i < n_ctx_tiles
        sections = [
            (0, dhq_f[...] + dhq_b[...]), (HW, dhi_f[...] + dhi_b[...]), (2 * HW, dhf_f[...]), (3 * HW, dhf_b[...]),
            (4 * HW, dhgate[...]), (5 * HW, dgq_f[...] + dgq_b[...]), (6 * HW, dgk_f[...] + dgk_b[...]),
            (7 * HW, dgv_f[...] + dgv_b[...]), (8 * HW, dggate[...]),
            (9 * HW, dghg[:, 0:HW]), (10 * HW, dghg[:, HW:2 * HW]),
            (11 * HW, dggla[:, 0:HW]), (12 * HW, dggla[:, HW:2 * HW]), (OFF_LR, dlr_f[...] + dlr_b[...])]
        dh = jnp.zeros((TM, d), F32)
        for off, val in sections:
            w = val.shape[1]
            vb = val.astype(BF16)
            dp_ref[:, off:off + w] = vb
            dh = dh + _dot_nt(vb, w_ref[:, off:off + w])
        n, r = _rms(z_ref[...])
        pre = pre_ref[...]
        scale = jnp.where(is_ctx, modc_ref[1:2, :], modx_ref[1:2, :])
        nw = n * pre
        dshift = _colsum(dh)
        dscale = _colsum(dh * nw)
        dnw = dh * (1.0 + scale)
        dpre = _colsum(dnw * n)
        gx_ref[...] = dz2_ref[...] + _rms_bwd(dnw * pre, n, r)
        zero = jnp.zeros((1, d), F32)

        @pl.when(i == 0)
        def _():
            stat_ref[...] = jnp.zeros_like(stat_ref)

        stat_ref[0:1, :] += jnp.where(is_ctx, zero, dshift)
        stat_ref[1:2, :] += jnp.where(is_ctx, zero, dscale)
        stat_ref[2:3, :] += jnp.where(is_ctx, dshift, zero)
        stat_ref[3:4, :] += jnp.where(is_ctx, dscale, zero)
        stat_ref[4:5, :] += dpre

    full = lambda i: (i, 0)
    lat = lambda i: (jnp.maximum(i - n_ctx_tiles, 0), 0)
    fixed = lambda i: (0, 0)
    piece_specs = [pl.BlockSpec((TM, a.shape[1]), full) for a in pieces]
    in_specs = [pl.BlockSpec((TM, d), full), pl.BlockSpec((TM, d), lat), pl.BlockSpec((8, d), fixed),
                pl.BlockSpec((8, d), fixed), pl.BlockSpec((1, d), fixed), VMEM_SPEC] + piece_specs
    return pl.pallas_call(
        body, name=name, grid=(rows // TM,),
        out_shape=[jax.ShapeDtypeStruct((rows, width), BF16), jax.ShapeDtypeStruct((lat_rows, d), F32),
                   jax.ShapeDtypeStruct((8, d), F32)],
        in_specs=in_specs,
        out_specs=[pl.BlockSpec((TM, width), full), pl.BlockSpec((TM, d), lat), pl.BlockSpec((8, d), fixed)],
        compiler_params=_cparams(dimension_semantics=("arbitrary",)),
    )(z, dz2, modc, modx, pre1, w_r, *pieces)


def _weight_grad(xs, dy, name, col_block=None, tk=256, tn=512):
    rows = dy.shape[0]
    k = xs.shape[1]
    if col_block is None:
        n, tn_, cb = dy.shape[1], tn, 0
    else:
        n, tn_, cb = col_block[0], col_block[0], col_block[1]
    tn_ = min(tn_, n)
    tk_ = min(tk, k)

    def body(x_ref, dy_ref, o_ref):
        o_ref[...] = _dot_tn(x_ref[...], dy_ref[...])

    return pl.pallas_call(
        body, name=name, grid=(k // tk_, n // tn_),
        out_shape=jax.ShapeDtypeStruct((k, n), F32),
        in_specs=[pl.BlockSpec((rows, tk_), lambda i, j: (0, i)),
                  pl.BlockSpec((rows, tn_), lambda i, j: (0, j + cb))],
        out_specs=pl.BlockSpec((tk_, tn_), lambda i, j: (i, j)),
        compiler_params=_cparams(dimension_semantics=("parallel", "parallel")),
    )(xs, dy)


def _chunk_terms(q, k, g, fw):
    c = CHUNK
    r = lax.broadcasted_iota(jnp.int32, (c, c), 0)
    s = lax.broadcasted_iota(jnp.int32, (c, c), 1)
    causal = (s <= r) if fw else (s >= r)
    causal_t = (s >= r) if fw else (s <= r)
    cum = jnp.dot(causal.astype(F32), g, precision=HIGHEST, preferred_element_type=F32)
    row = lax.broadcasted_iota(jnp.int32, (c, 1), 0)
    pos = row if fw else (c - 1 - row)
    starts = [None]
    for j in range(1, NSUB):
        rj = SUB * j - 1 if fw else c - SUB * j
        starts.append(cum[rj:rj + 1, :])
    in_blk = [(pos >= SUB * j) & (pos < SUB * (j + 1)) for j in range(NSUB)]
    e = [jnp.exp(cum)]
    for j in range(1, NSUB):
        e.append(jnp.exp(jnp.where(pos >= SUB * j, cum - starts[j], -1e30)))
    own = jnp.zeros_like(cum)
    for j in range(1, NSUB):
        own = own + jnp.where(in_blk[j], starts[j], 0.0)
    kscale = jnp.exp(own - cum)
    rend = c - 1 if fw else 0
    cend = cum[rend:rend + 1, :]
    tail = jnp.exp(cend - cum)
    qcat = jnp.concatenate([q * e[j] for j in range(NSUB)], axis=1).astype(BF16)
    kt = k * kscale
    km = jnp.concatenate([jnp.where(in_blk[j], kt, 0.0) for j in range(NSUB)], axis=1).astype(BF16)
    return dict(causal=causal, causal_t=causal_t, e=e, in_blk=in_blk, kscale=kscale, cend=cend, tail=tail,
                qcat=qcat, km=km)


def _chunk_fwd(q, k, v, g, st0, fw):
    t = _chunk_terms(q, k, g, fw)
    a = jnp.where(t["causal"], _dot_nt(t["qcat"], t["km"]), 0.0)
    o = _dot(a, v) + _dot_nt(t["qcat"][:, 0:HD], st0)
    st1 = st0 * jnp.exp(t["cend"]) + _dot_tn(v, k * t["tail"])
    return o, st1


def _chunk_bwd(q, k, v, g, st0, do, dst1, fw):
    t = _chunk_terms(q, k, g, fw)
    qcat, km, e = t["qcat"], t["km"], t["e"]
    a_t = jnp.where(t["causal_t"], _dot_nt(km, qcat), 0.0)
    ktail = k * t["tail"]
    dv = _dot(a_t, do) + _dot_nt(ktail, dst1)
    da = jnp.where(t["causal"], _dot_nt(do, v), 0.0)
    da_t = jnp.where(t["causal_t"], _dot_nt(v, do), 0.0)
    dqcat = _dot(da, km)
    dq_inter = e[0] * _dot(do, st0)
    dq = dq_inter
    for j in range(NSUB):
        dq = dq + e[j] * dqcat[:, j * HD:(j + 1) * HD]
    dkm = _dot(da_t, qcat)
    dkt = jnp.zeros_like(k)
    for j in range(NSUB):
        dkt = dkt + jnp.where(t["in_blk"][j], dkm[:, j * HD:(j + 1) * HD], 0.0)
    dk_inter = _dot(v, dst1) * t["tail"]
    dk = dkt * t["kscale"] + dk_inter
    dcum = q * dq_inter - k * dk_inter
    for j in range(NSUB):
        sl = slice(j * HD, (j + 1) * HD)
        dcum = dcum + qcat[:, sl].astype(F32) * dqcat[:, sl] - km[:, sl].astype(F32) * dkm[:, sl]
    ecend = jnp.exp(t["cend"])
    end = ecend * _colsum(st0 * dst1) + _colsum(k * dk_inter)
    dg = jnp.dot(t["causal_t"].astype(F32), dcum, precision=HIGHEST, preferred_element_type=F32) + end
    dst0 = dst1 * ecend + _dot_tn(do, q * e[0])
    return dq, dk, dv, dg, dst0


def _chunk_index(step, n_ctx_chunks, n_chunks, fw):
    if fw:
        return step
    return jnp.where(step < n_ctx_chunks, n_ctx_chunks - 1 - step, n_chunks - 1 + n_ctx_chunks - step)


def _hg_inputs(hq, hf, lbv, d_idx, sl):
    lb = _sigmoid(lbv[d_idx:d_idx + 1, sl] - lbv[2 + d_idx:3 + d_idx, sl])
    sg = _sigmoid(hf)
    f = lb + (1.0 - lb) * sg
    return _silu(hq), 1.0 - f, jnp.log(f), f, sg, lb


def _scan_fwd(p, side, n_ctx_chunks, fw, branch, name):
    rows = p.shape[0]
    n_chunks = rows // CHUNK
    d_idx = 0 if fw else 1
    hg = branch == "hg"
    cols = (C_HQ, C_HI, C_HF_FW + d_idx) if hg else (C_GQ, C_GK, C_GV)

    def body(*refs):
        if hg:
            a_ref, b_ref, c_ref, lb_ref, o_ref, st_ref, state = refs
        else:
            a_ref, b_ref, c_ref, lr_ref, wgk_ref, bgk_ref, o_ref, st_ref, state = refs
            logits = _dot(lr_ref[...], wgk_ref[...]) + bgk_ref[...]
            g_all = _log_sigmoid(logits) * (1.0 / GATE_NORM)

        @pl.when(pl.program_id(0) == 0)
        def _():
            state[...] = jnp.zeros_like(state)

        for h in range(NH):
            sl = slice(h * HD, (h + 1) * HD)
            if hg:
                q, k, g, _, _, _ = _hg_inputs(a_ref[:, sl], c_ref[:, sl], lb_ref[...], d_idx, sl)
                v = b_ref[:, sl]
            else:
                q, k, v, g = a_ref[:, sl] * (HD ** -0.5), b_ref[:, sl], c_ref[:, sl], g_all[:, sl]
            st0 = state[h]
            st_ref[0, h] = st0
            o, st1 = _chunk_fwd(q, k, v, g, st0, fw)
            o_ref[:, sl] = o
            state[h] = st1

    def cmap(blk):
        return pl.BlockSpec((CHUNK, HW), lambda j: (_chunk_index(j, n_ctx_chunks, n_chunks, fw), blk))

    fixed = lambda j: (0, 0)
    in_specs = [cmap(cols[0]), cmap(cols[1]), cmap(cols[2])]
    if hg:
        in_specs += [pl.BlockSpec((4, HW), fixed)]
        args = (p, p, p, side)
    else:
        in_specs += [pl.BlockSpec((CHUNK, 128), lambda j: (_chunk_index(j, n_ctx_chunks, n_chunks, fw), OFF_LR // 128)),
                     pl.BlockSpec((128, HW), fixed), pl.BlockSpec((1, HW), fixed)]
        args = (p, p, p, p, side[0], side[1])
    return pl.pallas_call(
        body, name=name, grid=(n_chunks,),
        out_shape=[jax.ShapeDtypeStruct((rows, HW), F32), jax.ShapeDtypeStruct((n_chunks, NH, HD, HD), F32)],
        in_specs=in_specs,
        out_specs=[pl.BlockSpec((CHUNK, HW), lambda j: (_chunk_index(j, n_ctx_chunks, n_chunks, fw), 0)),
                   pl.BlockSpec((1, NH, HD, HD), lambda j: (_chunk_index(j, n_ctx_chunks, n_chunks, fw), 0, 0, 0))],
        scratch_shapes=[pltpu.VMEM((NH, HD, HD), F32)],
        compiler_params=_cparams(dimension_semantics=("arbitrary",)),
    )(*args)


def _scan_bwd(p, side, states, d_o, n_ctx_chunks, fw, branch, name):
    rows = p.shape[0]
    n_chunks = rows // CHUNK
    d_idx = 0 if fw else 1
    hg = branch == "hg"
    cols = (C_HQ, C_HI, C_HF_FW + d_idx) if hg else (C_GQ, C_GK, C_GV)

    def body(*refs):
        if hg:
            a_ref, b_ref, c_ref, lb_ref, st_ref, do_ref, da_ref, db_ref, dc_ref, dlb_ref, dstate = refs
        else:
            (a_ref, b_ref, c_ref, lr_ref, wgk_ref, bgk_ref, st_ref, do_ref, da_ref, db_ref, dc_ref, dlr_ref,
             dwgk_ref, dbias_ref, dstate) = refs
            lr = lr_ref[...]
            logits = _dot(lr, wgk_ref[...]) + bgk_ref[...]
            g_all = _log_sigmoid(logits) * (1.0 / GATE_NORM)

        @pl.when(pl.program_id(0) == 0)
        def _():
            dstate[...] = jnp.zeros_like(dstate)
            if hg:
                dlb_ref[...] = jnp.zeros_like(dlb_ref)
            else:
                dwgk_ref[...] = jnp.zeros_like(dwgk_ref)
                dbias_ref[...] = jnp.zeros_like(dbias_ref)

        dg_parts = []
        for h in range(NH):
            sl = slice(h * HD, (h + 1) * HD)
            if hg:
                hq, hf = a_ref[:, sl], c_ref[:, sl]
                q, k, g, f, sg, lb = _hg_inputs(hq, hf, lb_ref[...], d_idx, sl)
                v = b_ref[:, sl]
            else:
                q, k, v, g = a_ref[:, sl] * (HD ** -0.5), b_ref[:, sl], c_ref[:, sl], g_all[:, sl]
            dq, dk, dv, dg, dst0 = _chunk_bwd(q, k, v, g, st_ref[0, h], do_ref[:, sl], dstate[h], fw)
            dstate[h] = dst0
            if hg:
                da_ref[:, sl] = dq * _dsilu(hq)
                db_ref[:, sl] = dv
                df = dg / f - dk
                dc_ref[:, sl] = df * (1.0 - lb) * sg * (1.0 - sg)
                dlb_ref[0:1, sl] += _colsum(df * (1.0 - sg))
            else:
                da_ref[:, sl] = dq * (HD ** -0.5)
                db_ref[:, sl] = dk
                dc_ref[:, sl] = dv
                dg_parts.append(dg)
        if not hg:
            dlogits = jnp.concatenate(dg_parts, axis=1) * (1.0 / GATE_NORM) * (1.0 - _sigmoid(logits))
            dlr_ref[...] = _dot_nt(dlogits, wgk_ref[...])
            dwgk_ref[...] += _dot_tn(lr, dlogits)
            dbias_ref[0:1, :] += _colsum(dlogits)

    def chunk_of(j):
        return _chunk_index(n_chunks - 1 - j, n_ctx_chunks, n_chunks, fw)

    def cmap(blk, width=HW):
        return pl.BlockSpec((CHUNK, width), lambda j: (chunk_of(j), blk))

    fixed = lambda j: (0, 0)
    st_spec = pl.BlockSpec((1, NH, HD, HD), lambda j: (chunk_of(j), 0, 0, 0))
    big = jax.ShapeDtypeStruct((rows, HW), F32)
    if hg:
        in_specs = [cmap(cols[0]), cmap(cols[1]), cmap(cols[2]), pl.BlockSpec((4, HW), fixed), st_spec, cmap(0)]
        args = (p, p, p, side, states, d_o)
        out_shape = [big, big, big, jax.ShapeDtypeStruct((8, HW), F32)]
        out_specs = [cmap(0), cmap(0), cmap(0), pl.BlockSpec((8, HW), fixed)]
    else:
        in_specs = [cmap(cols[0]), cmap(cols[1]), cmap(cols[2]), cmap(OFF_LR // 128, 128),
                    pl.BlockSpec((128, HW), fixed), pl.BlockSpec((1, HW), fixed), st_spec, cmap(0)]
        args = (p, p, p, p, side[0], side[1], states, d_o)
        out_shape = [big, big, big, jax.ShapeDtypeStruct((rows, 128), F32), jax.ShapeDtypeStruct((128, HW), F32),
                     jax.ShapeDtypeStruct((8, HW), F32)]
        out_specs = [cmap(0), cmap(0), cmap(0), cmap(0, 128), pl.BlockSpec((128, HW), fixed),
                     pl.BlockSpec((8, HW), fixed)]
    return pl.pallas_call(
        body, name=name, grid=(n_chunks,), out_shape=out_shape, in_specs=in_specs, out_specs=out_specs,
        scratch_shapes=[pltpu.VMEM((NH, HD, HD), F32)],
        compiler_params=_cparams(dimension_semantics=("arbitrary",)),
    )(*args)


SMALL_ROWS = 56
ROWS_MOD_X = (0, 1, 8, 16, 17, 18)
ROWS_MOD_C = (2, 3)
ROW_PRE1, ROW_POST1, ROW_ONORM, ROW_PRE2, ROW_POST2, ROW_LB, ROW_BGK, ROW_WGK = 4, 9, 10, 19, 20, 24, 32, 40


def _reduce_small(gathered, lb_full, name):
    _, _, d = gathered.shape

    def body(g_ref, lb_ref, sum_ref, dmod_ref, dbmod_ref, dlb_ref):
        total = g_ref[0]
        for b in range(1, N_DEV):
            total = total + g_ref[b]
        sum_ref[...] = total
        dmod_ref[...] = jnp.zeros_like(dmod_ref)
        for m in range(N_MOD):
            col = slice(m * d, (m + 1) * d)
            acc = jnp.zeros((1, d), F32)
            for b in range(N_DEV):
                row = g_ref[b, ROWS_MOD_X[m]:ROWS_MOD_X[m] + 1, :]
                dmod_ref[b:b + 1, col] = row
                acc = acc + row
            if m < 2:
                ctx_row = total[ROWS_MOD_C[m]:ROWS_MOD_C[m] + 1, :]
                dmod_ref[8:9, col] = ctx_row
                acc = acc + ctx_row
            dbmod_ref[:, col] = acc
        lbv = lb_ref[...]
        for dd in range(2):
            lb = _sigmoid(lbv[dd:dd + 1, :] - lbv[2 + dd:3 + dd, :])
            gl = total[ROW_LB:ROW_LB + 1, dd * HW:(dd + 1) * HW] * lb * (1.0 - lb)
            dlb_ref[dd:dd + 1, :] = gl
            dlb_ref[2 + dd:3 + dd, :] = -gl

    return pl.pallas_call(
        body, name=name,
        out_shape=[jax.ShapeDtypeStruct((SMALL_ROWS, d), F32), jax.ShapeDtypeStruct((16, N_MOD * d), F32),
                   jax.ShapeDtypeStruct((1, N_MOD * d), F32), jax.ShapeDtypeStruct((4, HW), F32)],
        in_specs=[VMEM_SPEC] * 2, out_specs=[VMEM_SPEC] * 4, compiler_params=_cparams(),
    )(gathered, lb_full)


def _c_ctx_grad(gathered, c_ctx_row, name):
    def body(g_ref, c_ref, o_ref):
        acc = g_ref[0, 0:1, :]
        for chip in range(1, N_CHIP):
            acc = acc + g_ref[2 * chip, 0:1, :]
        o_ref[...] = acc * _dsilu(c_ref[...])

    return pl.pallas_call(
        body, name=name, out_shape=jax.ShapeDtypeStruct(c_ctx_row.shape, F32),
        in_specs=[VMEM_SPEC] * 2, out_specs=VMEM_SPEC, compiler_params=_cparams(),
    )(gathered, c_ctx_row)


def _relayout_w_in(w):
    pad = jnp.zeros((w.shape[0], 128 - 2 * RANK), w.dtype)
    return jnp.concatenate([w[:, :9 * HW], w[:, 9 * HW + 2 * RANK:], w[:, 9 * HW:9 * HW + 2 * RANK], pad], axis=1)


def _unrelayout_w_in(g_main, g_lr):
    return jnp.concatenate([g_main[:, :9 * HW], g_lr[:, :2 * RANK], g_main[:, 9 * HW:]], axis=1)


def _blocked(full, n_blocks):
    k, n = full.shape
    return full.reshape(k, n_blocks, n // n_blocks).transpose(1, 0, 2)


def _unblocked(blocks):
    nb, k, n = blocks.shape
    return blocks.transpose(1, 0, 2).reshape(k, nb * n)


def _sample_front(x0, ctx0, modc, modx, norm_pre1, lb_full, gla_side, w_in_r):
    ctx_len = ctx0.shape[0]
    n_ctx_tiles = ctx_len // TM
    n_ctx_chunks = ctx_len // CHUNK
    z = jnp.concatenate([ctx0, x0], axis=0)
    h1, p = _in_projection(z, modc, modx, norm_pre1, w_in_r, n_ctx_tiles, "in_projection")
    o_hg_fw, st_hg_fw = _scan_fwd(p, lb_full, n_ctx_chunks, True, "hg", "scan_hg_fw")
    o_hg_bw, st_hg_bw = _scan_fwd(p, lb_full, n_ctx_chunks, False, "hg", "scan_hg_bw")
    o_gla_fw, st_gla_fw = _scan_fwd(p, gla_side[0], n_ctx_chunks, True, "gla", "scan_gla_fw")
    o_gla_bw, st_gla_bw = _scan_fwd(p, gla_side[1], n_ctx_chunks, False, "gla", "scan_gla_bw")
    return dict(z=z, h1=h1, p=p, o_list=[o_hg_fw, o_hg_bw, o_gla_fw, o_gla_bw],
                states=[st_hg_fw, st_hg_bw, st_gla_fw, st_gla_bw])


def _sample_back(front, x0, ctx0, target0, modc, modx, norm_pre1, norms, onorms, lb_full, gla_side, w_in_r, wbh, wbg,
                 wout, wg, wu, wd):
    seq, d = x0.shape
    ctx_len = ctx0.shape[0]
    n_ctx_tiles = ctx_len // TM
    n_tiles = (ctx_len + seq) // TM
    n_ctx_chunks = ctx_len // CHUNK
    z, h1, p, o_list = front["z"], front["h1"], front["p"], front["o_list"]
    st_hg_fw, st_hg_bw, st_gla_fw, st_gla_bw = front["states"]
    (loss_part, dz2, y1, merged, og_hg, og_gla, h2, a_act, du, dv, dy2, stat_ffn) = _mixer_ffn(
        x0, p, o_list, modx, norms, onorms, wbh, wbg, wout, wg, wu, wd, target0, n_ctx_tiles, "mixer_ffn")

    (d_ohg, d_ogla, d_hgate, d_ggate, d_ghg, d_ggla, dy1, db_hg, db_gla, stat_mix) = _mixer_tail_bwd(
        x0, p, o_list, dz2, y1, modx, norms, onorms, wbh, wbg, wout, n_ctx_tiles, n_tiles, "mixer_tail_bwd")
    dhq_f, dhi_f, dhf_f, dlb_f = _scan_bwd(p, lb_full, st_hg_fw, d_ohg, n_ctx_chunks, True, "hg", "scan_hg_fw_bwd")
    dhq_b, dhi_b, dhf_b, dlb_b = _scan_bwd(p, lb_full, st_hg_bw, d_ohg, n_ctx_chunks, False, "hg", "scan_hg_bw_bwd")
    dgq_f, dgk_f, dgv_f, dlr_f, dwgk_f, dbgk_f = _scan_bwd(p, gla_side[0], st_gla_fw, d_ogla, n_ctx_chunks, True, "gla",
                                                           "scan_gla_fw_bwd")
    dgq_b, dgk_b, dgv_b, dlr_b, dwgk_b, dbgk_b = _scan_bwd(p, gla_side[1], st_gla_bw, d_ogla, n_ctx_chunks, False, "gla",
                                                           "scan_gla_bw_bwd")
    pieces = [dhq_f, dhq_b, dhi_f, dhi_b, dhf_f, dhf_b, d_hgate, dgq_f, dgq_b, dgk_f, dgk_b, dgv_f, dgv_b, d_ggate,
              d_ghg, d_ggla, dlr_f, dlr_b]
    dp, grad_x, stat_in = _in_projection_bwd(z, dz2, modc, modx, norm_pre1, w_in_r, pieces, n_ctx_tiles,
                                             "in_projection_bwd")

    g_in_main = _weight_grad_cols(h1, dp, OFF_LR, "grad_w_in_main")
    g_in_lr = _weight_grad(h1, dp, "grad_w_in_lr", col_block=(128, OFF_LR // 128))
    dff = wg.shape[1]
    tn_ff = dff // N_CHIP if (dff // N_CHIP) % 128 == 0 else 256
    return dict(
        loss_part=loss_part, grad_x=grad_x, stat_in=stat_in, stat_mix=stat_mix, stat_ffn=stat_ffn,
        dlb=(dlb_f, dlb_b), dwgk=(dwgk_f, dwgk_b), dbgk=(dbgk_f, dbgk_b),
        g_w_in=_unrelayout_w_in(g_in_main, g_in_lr),
        g_br_hg=_weight_grad(og_hg, db_hg, "grad_w_br_hg"),
        g_br_gla=_weight_grad(og_gla, db_gla, "grad_w_br_gla"),
        g_out=_weight_grad(merged, dy1, "grad_w_out"),
        g_gate=_weight_grad(h2, du, "grad_w_ff_gate", tn=tn_ff),
        g_up=_weight_grad(h2, dv, "grad_w_ff_up", tn=tn_ff),
        g_down=_weight_grad(a_act, dy2, "grad_w_ff_down", tk=tn_ff))


def kernel(x, c, ctx, c_ctx, w_mod, b_mod, norm_pre1, norm_post1, norm_pre2, norm_post2, w_in, hg_lb, hg_onorm, gla_w_gk, gla_b_gk, gla_onorm, w_br_hg, w_br_gla, w_out, w_ff_gate, w_ff_up, w_ff_down, loss_target, m_c_ctx, m_w_mod, m_b_mod, m_norm_pre1, m_norm_post1, m_norm_pre2, m_norm_post2, m_w_in, m_hg_lb, m_hg_onorm, m_gla_w_gk, m_gla_b_gk, m_gla_onorm, m_w_br_hg, m_w_br_gla, m_w_out, m_w_ff_gate, m_w_ff_up, m_w_ff_down, v_c_ctx, v_w_mod, v_b_mod, v_norm_pre1, v_norm_post1, v_norm_pre2, v_norm_post2, v_w_in, v_hg_lb, v_hg_onorm, v_gla_w_gk, v_gla_b_gk, v_gla_onorm, v_w_br_hg, v_w_br_gla, v_w_out, v_w_ff_gate, v_w_ff_up, v_w_ff_down):
    seq, d = x.shape[1], x.shape[2]
    ctx_len = ctx.shape[1]
    assert seq % TM == 0 and ctx_len % TM == 0 and d == 2 * HW
    ax, ay, ac = lax.axis_index("x"), lax.axis_index("y"), lax.axis_index("c")
    chip = 2 * ax + ay
    dev = 2 * chip + ac
    c_arr = jnp.reshape(ac, (1,)).astype(jnp.int32)

    nc = d // 128
    pad8 = lambda a: jnp.pad(a, ((0, -a.shape[0] % 8), (0, 0)))
    small1 = jnp.concatenate([c.reshape(nc, 128), pad8(hg_lb.reshape(4, 128)), gla_w_gk.reshape(2 * RANK, 128),
                              pad8(gla_b_gk.reshape(2, 128))], axis=0)
    got1 = _allgather8(small1, "gather_small_params")
    c_all = got1[:, :nc, :].reshape(N_DEV, d)
    per_chip = got1[0::2]
    lb_full = per_chip[:, nc:nc + 4, :].transpose(1, 0, 2).reshape(4, HW)
    wgk_full = per_chip[:, nc + 8:nc + 8 + 2 * RANK, :].transpose(1, 0, 2).reshape(2, RANK, HW)
    bgk_full = per_chip[:, nc + 8 + 2 * RANK:nc + 10 + 2 * RANK, :].transpose(1, 0, 2).reshape(2, HW)
    wgk_pad = [jnp.zeros((128, HW), F32).at[dd * RANK:(dd + 1) * RANK].set(wgk_full[dd]) for dd in range(2)]
    bgk = [bgk_full[dd:dd + 1] for dd in range(2)]

    n_mod_cols = w_mod.shape[2]
    cond = jnp.concatenate([c_all, pad8(c_ctx.reshape(1, d))], axis=0)
    b_cols = lax.dynamic_slice(b_mod, (0, chip * n_mod_cols), (1, n_mod_cols))
    mod_part = _mod_forward(cond, w_mod[0], b_cols, "mod_forward")
    mod_got = _allgather8(mod_part, "gather_mod")
    mod_all = mod_got[0::2].transpose(1, 0, 2).reshape(16, N_CHIP * n_mod_cols)
    modx = pad8(lax.dynamic_slice(mod_all, (dev, 0), (1, N_MOD * d)).reshape(N_MOD, d))
    modc = pad8(mod_all[8].reshape(N_MOD, d))

    chip_arr = jnp.reshape(chip, (1,)).astype(jnp.int32)
    blocks = [_cast_into_blocks(chip_arr, w_[0], "cast_" + nm) for w_, nm in (
        (w_in, "w_in"), (w_br_hg, "w_br_hg"), (w_br_gla, "w_br_gla"), (w_out, "w_out"), (w_ff_gate, "w_ff_gate"),
        (w_ff_up, "w_ff_up"), (w_ff_down, "w_ff_down"))]
    gathered_in = _gather_blocks(blocks[:1], "gather_w_in")
    sems, lands, token = _blocks_start(blocks[1:], "gather_rest_start")
    w_in_r = _relayout_w_in(_unblocked(gathered_in[0]))

    norms = jnp.concatenate([norm_pre1, norm_post1, norm_pre2, norm_post2, jnp.zeros((4, d), F32)], axis=0)
    onorms = jnp.zeros((8, d), F32).at[0, :HD].set(hg_onorm[0]).at[1, :HD].set(gla_onorm[0])
    gla_side = [(wgk_pad[dd], bgk[dd]) for dd in range(2)]
    modx = modx + token[0, 0]
    front = _sample_front(x[0], ctx[0], modc, modx, norm_pre1, lb_full, gla_side, w_in_r)
    lands = _blocks_wait(sems, lands, front["o_list"][3], "gather_rest_wait")
    gathered = _blocks_finish(lands, "gather_rest_finish")
    wbh, wbg = _unblocked(gathered[0]), _unblocked(gathered[1])
    wout = gathered[2].reshape(d, d)
    wg, wu = _unblocked(gathered[3]), _unblocked(gathered[4])
    wd = gathered[5].reshape(wg.shape[1], d)
    r = _sample_back(front, x[0], ctx[0], loss_target[0], modc, modx, norm_pre1, norms, onorms, lb_full, gla_side,
                     w_in_r, wbh, wbg, wout, wg, wu, wd)
    loss_part, grad_x, stat_in, stat_mix, stat_ffn = (r[k] for k in ("loss_part", "grad_x", "stat_in", "stat_mix",
                                                                     "stat_ffn"))
    (dlb_f, dlb_b), (dwgk_f, dwgk_b), (dbgk_f, dbgk_b) = r["dlb"], r["dwgk"], r["dbgk"]
    dff = wg.shape[1]

    full = [_blocked(r["g_w_in"], N_CHIP), _blocked(r["g_br_hg"], N_CHIP), _blocked(r["g_br_gla"], N_CHIP),
            r["g_out"].reshape(N_CHIP, d // N_CHIP, d), _blocked(r["g_gate"], N_CHIP), _blocked(r["g_up"], N_CHIP),
            r["g_down"].reshape(N_CHIP, dff // N_CHIP, d)]
    from_sibling = _send_other_half(full, "grads_to_sibling")
    names = ["w_in", "w_br_hg", "w_br_gla", "w_out", "w_ff_gate", "w_ff_up", "w_ff_down"]
    pairs = [_pair_sum(c_arr, f, r, "pair_sum_" + nm) for f, r, nm in zip(full, from_sibling, names)]
    at_owner = _scatter_blocks(pairs, "grads_to_owner")
    own_half = [_sum_owner(chip_arr, pr, g, "chip_sum_" + nm) for pr, g, nm in zip(pairs, at_owner, names)]
    other_half = _swap_with_sibling(own_half, "halves_to_sibling")
    big_w = [w_in, w_br_hg, w_br_gla, w_out, w_ff_gate, w_ff_up, w_ff_down]
    big_m = [m_w_in, m_w_br_hg, m_w_br_gla, m_w_out, m_w_ff_gate, m_w_ff_up, m_w_ff_down]
    big_v = [v_w_in, v_w_br_hg, v_w_br_gla, v_w_out, v_w_ff_gate, v_w_ff_up, v_w_ff_down]
    big = {}
    for nm, own, oth, w_, m_, v_ in zip(names, own_half, other_half, big_w, big_m, big_v):
        res = _adamw_halves(c_arr, own, oth, w_[0], m_[0], v_[0], "adamw_" + nm)
        big[nm] = [r[None] for r in res]

    small2 = jnp.concatenate([
        stat_in, stat_mix, stat_ffn, jnp.concatenate([dlb_f, dlb_b], axis=1), jnp.concatenate([dbgk_f, dbgk_b], axis=1),
        jnp.concatenate([dwgk_f[0:RANK], dwgk_b[RANK:2 * RANK]], axis=1)], axis=0)
    assert small2.shape[0] == SMALL_ROWS
    got2 = _allgather8(small2, "gather_small_grads")
    total, dmod_all, g_b_mod, g_lb_full = _reduce_small(got2, lb_full, "reduce_small")
    dmod_cols = lax.dynamic_slice(dmod_all, (0, chip * n_mod_cols), (16, n_mod_cols))
    g_w_mod, cctx_part = _mod_backward(cond, w_mod[0], dmod_cols, "mod_backward")
    got3 = _allgather8(cctx_part, "gather_c_ctx_grad")
    g_c_ctx = _c_ctx_grad(got3, c_ctx.reshape(1, d), "c_ctx_grad")

    g_pre1, g_post1, g_pre2, g_post2 = (total[r_:r_ + 1] for r_ in (ROW_PRE1, ROW_POST1, ROW_PRE2, ROW_POST2))
    g_hg_on, g_gla_on = total[ROW_ONORM:ROW_ONORM + 1, 0:HD], total[ROW_ONORM:ROW_ONORM + 1, HD:2 * HD]
    n_lb = hg_lb.shape[2]
    g_hg_lb = lax.dynamic_slice(g_lb_full, (0, chip * n_lb), (4, n_lb))
    g_bgk = lax.dynamic_slice(total[ROW_BGK:ROW_BGK + 1].reshape(2, HW), (0, chip * n_lb), (2, n_lb))
    g_wgk_full = total[ROW_WGK:ROW_WGK + RANK].reshape(RANK, 2, HW).transpose(1, 0, 2).reshape(2 * RANK, HW)
    g_wgk = lax.dynamic_slice(g_wgk_full, (0, chip * n_lb), (2 * RANK, n_lb))

    small_items = [
        (g_c_ctx, c_ctx.reshape(1, d), m_c_ctx.reshape(1, d), v_c_ctx.reshape(1, d)),
        (g_b_mod, b_mod, m_b_mod, v_b_mod),
        (g_pre1, norm_pre1, m_norm_pre1, v_norm_pre1),
        (g_post1, norm_post1, m_norm_post1, v_norm_post1),
        (g_pre2, norm_pre2, m_norm_pre2, v_norm_pre2),
        (g_post2, norm_post2, m_norm_post2, v_norm_post2),
        (g_hg_lb, hg_lb.reshape(4, n_lb), m_hg_lb.reshape(4, n_lb), v_hg_lb.reshape(4, n_lb)),
        (g_hg_on, hg_onorm, m_hg_onorm, v_hg_onorm),
        (g_wgk, gla_w_gk.reshape(2 * RANK, n_lb), m_gla_w_gk.reshape(2 * RANK, n_lb), v_gla_w_gk.reshape(2 * RANK, n_lb)),
        (g_bgk, gla_b_gk.reshape(2, n_lb), m_gla_b_gk.reshape(2, n_lb), v_gla_b_gk.reshape(2, n_lb)),
        (g_gla_on, gla_onorm, m_gla_onorm, v_gla_onorm),
    ]
    small_res = _adamw_whole(small_items, "adamw_small")
    mod_res = _adamw_tiled(g_w_mod, w_mod[0], m_w_mod[0], v_w_mod[0], "adamw_w_mod")

    loss = lax.psum(loss_part[0, 0], ("x", "y", "c"))

    shapes = dict(c_ctx=c_ctx.shape, b_mod=b_mod.shape, norm_pre1=norm_pre1.shape, norm_post1=norm_post1.shape,
                  norm_pre2=norm_pre2.shape, norm_post2=norm_post2.shape, hg_lb=hg_lb.shape, hg_onorm=hg_onorm.shape,
                  gla_w_gk=gla_w_gk.shape, gla_b_gk=gla_b_gk.shape, gla_onorm=gla_onorm.shape)
    small_names = ["c_ctx", "b_mod", "norm_pre1", "norm_post1", "norm_pre2", "norm_post2", "hg_lb", "hg_onorm",
                   "gla_w_gk", "gla_b_gk", "gla_onorm"]
    grads, deltas, new_m, new_v = {}, {}, {}, {}
    for nm, item, res in zip(small_names, small_items, small_res):
        grads[nm] = item[0].reshape(shapes[nm])
        deltas[nm], new_m[nm], new_v[nm] = (r.reshape(shapes[nm]) for r in res)
    grads["w_mod"] = g_w_mod[None]
    deltas["w_mod"], new_m["w_mod"], new_v["w_mod"] = (r[None] for r in mod_res)
    for nm in names:
        grads[nm], deltas[nm], new_m[nm], new_v[nm] = big[nm]
    order = ["c_ctx", "w_mod", "b_mod", "norm_pre1", "norm_post1", "norm_pre2", "norm_post2", "w_in", "hg_lb",
             "hg_onorm", "gla_w_gk", "gla_b_gk", "gla_onorm", "w_br_hg", "w_br_gla", "w_out", "w_ff_gate", "w_ff_up",
             "w_ff_down"]
    return (loss, grad_x[None], *[grads[n] for n in order], *[deltas[n] for n in order],
            *[new_m[n] for n in order], *[new_v[n] for n in order])


def _weight_grad_cols(xs, dy, n_cols, name, tk=256, tn=512):
    rows = dy.shape[0]
    k = xs.shape[1]

    def body(x_ref, dy_ref, o_ref):
        o_ref[...] = _dot_tn(x_ref[...], dy_ref[...])

    return pl.pallas_call(
        body, name=name, grid=(k // tk, n_cols // tn),
        out_shape=jax.ShapeDtypeStruct((k, n_cols), F32),
        in_specs=[pl.BlockSpec((rows, tk), lambda i, j: (0, i)), pl.BlockSpec((rows, tn), lambda i, j: (0, j))],
        out_specs=pl.BlockSpec((tk, tn), lambda i, j: (i, j)),
        compiler_params=_cparams(dimension_semantics=("parallel", "parallel")),
    )(xs, dy)
```

```python
import functools

import jax
import jax.numpy as jnp
from jax import lax
from jax.experimental import pallas as pl
from jax.experimental.pallas import tpu as pltpu

F32 = jnp.float32
BF16 = jnp.bfloat16
HIGHEST = lax.Precision.HIGHEST
MESH = pl.DeviceIdType.MESH

EPS = 1e-6
CHUNK = 64
SUB = 16
NSUB = CHUNK // SUB
NH = 4
HD = 128
HW = NH * HD
RANK = 16
GATE_NORM = 16.0
N_MOD = 6
TM = 256
TM_FFN = 128
N_DEV = 8
N_CHIP = 4
VMEM_LIMIT = 56 * 1024 * 1024

ADAM_LR = 0.001
ADAM_B1 = 0.9
ADAM_B2 = 0.999
ADAM_EPS = 1e-08
ADAM_WD = 0.01
ADAM_STEP = 10

VMEM_SPEC = pl.BlockSpec(memory_space=pltpu.VMEM)
ANY_SPEC = pl.BlockSpec(memory_space=pl.ANY)
HBM_SPEC = pl.BlockSpec(memory_space=pltpu.HBM)
SEM_SPEC = pl.BlockSpec(memory_space=pltpu.SEMAPHORE)
EFFECT = pltpu.SideEffectType.DATAFLOW_SIDE_EFFECTING


def _cparams(**kw):
    return pltpu.CompilerParams(vmem_limit_bytes=VMEM_LIMIT, **kw)


def _dot(a, b):
    return jnp.dot(a.astype(BF16), b.astype(BF16), preferred_element_type=F32)


def _dot_nt(a, b):
    return lax.dot_general(a.astype(BF16), b.astype(BF16), (((1,), (1,)), ((), ())), preferred_element_type=F32)


def _dot_tn(a, b):
    return lax.dot_general(a.astype(BF16), b.astype(BF16), (((0,), (0,)), ((), ())), preferred_element_type=F32)


def _sigmoid(x):
    return 1.0 / (1.0 + jnp.exp(-x))


def _silu(x):
    return x * _sigmoid(x)


def _dsilu(x):
    s = _sigmoid(x)
    return s * (1.0 + x * (1.0 - s))


def _log_sigmoid(x):
    return jnp.minimum(x, 0.0) - jnp.log(1.0 + jnp.exp(-jnp.abs(x)))


def _colsum(a):
    return jnp.sum(a, axis=0, keepdims=True)


def _rms(a):
    r = lax.rsqrt(jnp.mean(a * a, axis=-1, keepdims=True) + EPS)
    return a * r, r


def _rms_bwd(dn, n, r):
    return r * (dn - n * jnp.mean(dn * n, axis=-1, keepdims=True))


def _place():
    x, y, c = lax.axis_index("x"), lax.axis_index("y"), lax.axis_index("c")
    chips = [(1 - x, y), (x, 1 - y), (1 - x, 1 - y)]
    return x, y, c, chips


def _allgather8(v, name):
    rows, cols = v.shape

    def body(x_ref, out_ref, send_sems, recv_sems, local_sem):
        x, y, c, chips = _place()
        me, sibling = (x, y, c), (x, y, 1 - c)

        def blk(px, py, pc):
            return out_ref.at[4 * px + 2 * py + pc]

        def copy(k, block, to, src=None):
            return pltpu.make_async_remote_copy(
                src_ref=blk(*block) if src is None else src, dst_ref=blk(*block),
                send_sem=send_sems.at[k], recv_sem=recv_sems.at[k], device_id=to, device_id_type=MESH)

        mine = pltpu.make_async_copy(x_ref, blk(*me), local_sem)
        mine.start()
        first = [copy(0, me, sibling, src=x_ref)]
        first += [copy(1 + j, me, (*chip, c), src=x_ref) for j, chip in enumerate(chips)]
        for cp in first:
            cp.start()
        passed = [copy(4 + j, (*chip, c), sibling) for j, chip in enumerate(chips)]
        for j, chip in enumerate(chips):
            copy(1 + j, (*chip, c), me).wait_recv()
            passed[j].start()
        copy(0, sibling, me).wait_recv()
        for j, chip in enumerate(chips):
            copy(4 + j, (*chip, 1 - c), me).wait_recv()
        for cp in first + passed:
            cp.wait_send()
        mine.wait()

    return pl.pallas_call(
        body, name=name,
        out_shape=jax.ShapeDtypeStruct((N_DEV, rows, cols), v.dtype),
        in_specs=[VMEM_SPEC], out_specs=VMEM_SPEC,
        scratch_shapes=[pltpu.SemaphoreType.DMA((7,)), pltpu.SemaphoreType.DMA((7,)), pltpu.SemaphoreType.DMA],
    )(v)


def _cast_into_blocks(chip_arr, w, name):
    rows, cols = w.shape
    tr = _row_tile(rows, 16, 256)

    def body(chip_ref, w_ref, o_ref):
        o_ref[0] = w_ref[...].astype(BF16)

    return pl.pallas_call(
        body, name=name,
        grid_spec=pltpu.PrefetchScalarGridSpec(
            num_scalar_prefetch=1, grid=(rows // tr,),
            in_specs=[pl.BlockSpec((tr, cols), lambda i, chip_ref: (i, 0))],
            out_specs=pl.BlockSpec((1, tr, cols), lambda i, chip_ref: (chip_ref[0], i, 0))),
        out_shape=jax.ShapeDtypeStruct((N_CHIP, rows, cols), BF16),
        compiler_params=_cparams(dimension_semantics=("parallel",)),
    )(chip_arr, w)


def _half_rows(ref, chip_id, pc):
    h = ref.shape[1] // 2
    return ref.at[chip_id, pl.ds(pl.multiple_of(pc * h, 8), h), :]


def _gather_blocks(lands, name):
    n = len(lands)

    def body(*refs):
        outs = refs[n:2 * n]
        send_sems, recv_sems = refs[2 * n:]
        x, y, c, chips = _place()
        me_chip = 2 * x + y
        sibling = (x, y, 1 - c)

        def copy(k, j, chip_id, pc, to):
            return pltpu.make_async_remote_copy(
                src_ref=_half_rows(outs[k], chip_id, pc), dst_ref=_half_rows(outs[k], chip_id, pc),
                send_sem=send_sems.at[k, j], recv_sem=recv_sems.at[k, j], device_id=to, device_id_type=MESH)

        started = []
        for k in range(n):
            for j, chip in enumerate(chips):
                cp = copy(k, j, me_chip, c, (*chip, c))
                cp.start()
                started.append(cp)
        for k in range(n):
            for j, (px, py) in enumerate(chips):
                copy(k, j, 2 * px + py, c, sibling).wait_recv()
                cp = copy(k, 3 + j, 2 * px + py, c, sibling)
                cp.start()
                started.append(cp)
        for k in range(n):
            for j, (px, py) in enumerate(chips):
                copy(k, 3 + j, 2 * px + py, 1 - c, sibling).wait_recv()
        for cp in started:
            cp.wait_send()

    return pl.pallas_call(
        body, name=name,
        out_shape=[jax.ShapeDtypeStruct(l.shape, l.dtype) for l in lands],
        in_specs=[ANY_SPEC] * n, out_specs=[ANY_SPEC] * n,
        input_output_aliases={i: i for i in range(n)},
        scratch_shapes=[pltpu.SemaphoreType.DMA((n, 6)), pltpu.SemaphoreType.DMA((n, 6))],
    )(*lands)


def _hbm(a):
    return pltpu.with_memory_space_constraint(a, pltpu.HBM)


def _blocks_start(lands, name):
    n = len(lands)
    n_sem = 3 * n

    def body(*refs):
        lnd = refs[:n]
        send_sems, recv_sems = refs[n:n + n_sem], refs[n + n_sem:n + 2 * n_sem]
        token = refs[-1]
        x, y, c, chips = _place()
        me_chip = 2 * x + y
        for k in range(n):
            for j, chip in enumerate(chips):
                pltpu.make_async_remote_copy(
                    src_ref=_half_rows(lnd[k], me_chip, c), dst_ref=_half_rows(lnd[k], me_chip, c),
                    send_sem=send_sems[3 * k + j], recv_sem=recv_sems[3 * k + j],
                    device_id=(*chip, c), device_id_type=MESH).start()
        token[...] = jnp.zeros_like(token)

    out = pl.pallas_call(
        body, name=name,
        out_shape=(*[pltpu.SemaphoreType.DMA(())] * (2 * n_sem),
                   *[pltpu.HBM(l.shape, l.dtype) for l in lands],
                   jax.ShapeDtypeStruct((8, 128), F32)),
        in_specs=[HBM_SPEC] * n,
        out_specs=(*[SEM_SPEC] * (2 * n_sem), *[HBM_SPEC] * n, VMEM_SPEC),
        input_output_aliases={i: 2 * n_sem + i for i in range(n)},
        compiler_params=pltpu.CompilerParams(has_side_effects=EFFECT),
    )(*[_hbm(l) for l in lands])
    return list(out[:2 * n_sem]), list(out[2 * n_sem:2 * n_sem + n]), out[-1]


def _blocks_wait(sems, lands, after, name):
    n = len(lands)
    n_sem = 3 * n

    def body(*refs):
        lnd = refs[:n]
        s_sems, r_sems = refs[n:n + n_sem], refs[n + n_sem:n + 2 * n_sem]
        x, y, c, chips = _place()
        me_chip = 2 * x + y
        for k in range(n):
            for j, (px, py) in enumerate(chips):
                cp = pltpu.make_async_remote_copy(
                    src_ref=_half_rows(lnd[k], me_chip, c), dst_ref=_half_rows(lnd[k], 2 * px + py, c),
                    send_sem=s_sems[3 * k + j], recv_sem=r_sems[3 * k + j],
                    device_id=(px, py, c), device_id_type=MESH)
                cp.wait_send()
                cp.wait_recv()

    out = pl.pallas_call(
        body, name=name,
        out_shape=tuple(pltpu.HBM(l.shape, l.dtype) for l in lands),
        in_specs=[HBM_SPEC] * n + [SEM_SPEC] * (2 * n_sem) + [ANY_SPEC],
        out_specs=[HBM_SPEC] * n,
        input_output_aliases={i: i for i in range(n)},
        compiler_params=pltpu.CompilerParams(has_side_effects=EFFECT),
    )(*lands, *sems, after)
    return list(out)


def _blocks_finish(lands, name):
    n = len(lands)

    def body(*refs):
        lnd = refs[n:2 * n]
        send_sems, recv_sems = refs[2 * n:]
        x, y, c, chips = _place()
        sibling = (x, y, 1 - c)

        def copy(k, j, chip_id, pc):
            return pltpu.make_async_remote_copy(
                src_ref=_half_rows(lnd[k], chip_id, pc), dst_ref=_half_rows(lnd[k], chip_id, pc),
                send_sem=send_sems.at[k, j], recv_sem=recv_sems.at[k, j], device_id=sibling, device_id_type=MESH)

        started = []
        for k in range(n):
            for j, (px, py) in enumerate(chips):
                cp = copy(k, j, 2 * px + py, c)
                cp.start()
                started.append(cp)
        for k in range(n):
            for j, (px, py) in enumerate(chips):
                copy(k, j, 2 * px + py, 1 - c).wait_recv()
        for cp in started:
            cp.wait_send()

    out = pl.pallas_call(
        body, name=name,
        out_shape=[jax.ShapeDtypeStruct(l.shape, l.dtype) for l in lands],
        in_specs=[ANY_SPEC] * n, out_specs=[ANY_SPEC] * n,
        input_output_aliases={i: i for i in range(n)},
        scratch_shapes=[pltpu.SemaphoreType.DMA((n, 3)), pltpu.SemaphoreType.DMA((n, 3))],
    )(*lands)
    return list(out)


def _gather_start(shards, name):
    n = len(shards)
    n_sem = 3 * n

    def body(*refs):
        ins, lands = refs[:n], refs[n:2 * n]
        send_sems, recv_sems = refs[2 * n:2 * n + n_sem], refs[2 * n + n_sem:2 * n + 2 * n_sem]
        token = refs[-1]
        x, y, c, chips = _place()
        me_chip = 2 * x + y
        for k in range(n):
            h = shards[k].shape[0] // 2
            rows = pl.ds(pl.multiple_of(c * h, 8), h)
            for j, chip in enumerate(chips):
                pltpu.make_async_remote_copy(
                    src_ref=ins[k].at[rows, :], dst_ref=lands[k].at[me_chip, rows, :],
                    send_sem=send_sems[3 * k + j], recv_sem=recv_sems[3 * k + j],
                    device_id=(*chip, c), device_id_type=MESH).start()
        token[...] = jnp.zeros_like(token)

    lands = [_hbm(lax.empty((N_CHIP,) + s.shape, s.dtype)) for s in shards]
    out = pl.pallas_call(
        body, name=name,
        out_shape=(*[pltpu.SemaphoreType.DMA(())] * (2 * n_sem),
                   *[pltpu.HBM(s.shape, s.dtype) for s in shards],
                   *[pltpu.HBM(l.shape, l.dtype) for l in lands],
                   jax.ShapeDtypeStruct((8, 128), F32)),
        in_specs=[HBM_SPEC] * (2 * n),
        out_specs=(*[SEM_SPEC] * (2 * n_sem), *[HBM_SPEC] * (2 * n), VMEM_SPEC),
        input_output_aliases={i: 2 * n_sem + i for i in range(2 * n)},
        compiler_params=pltpu.CompilerParams(has_side_effects=EFFECT),
    )(*[_hbm(s) for s in shards], *lands)
    sems = list(out[:2 * n_sem])
    return sems, list(out[2 * n_sem:2 * n_sem + n]), list(out[2 * n_sem + n:2 * n_sem + 2 * n]), out[-1]


def _gather_wait(sems, shards, lands, after, name):
    n = len(shards)
    n_sem = 3 * n

    def body(*refs):
        ins, lnd = refs[:n], refs[n:2 * n]
        s_sems, r_sems = refs[2 * n:2 * n + n_sem], refs[2 * n + n_sem:2 * n + 2 * n_sem]
        x, y, c, chips = _place()
        for k in range(n):
            h = shards[k].shape[0] // 2
            rows = pl.ds(pl.multiple_of(c * h, 8), h)
            for j, (px, py) in enumerate(chips):
                cp = pltpu.make_async_remote_copy(
                    src_ref=ins[k].at[rows, :], dst_ref=lnd[k].at[2 * px + py, rows, :],
                    send_sem=s_sems[3 * k + j], recv_sem=r_sems[3 * k + j],
                    device_id=(px, py, c), device_id_type=MESH)
                cp.wait_send()
                cp.wait_recv()

    out = pl.pallas_call(
        body, name=name,
        out_shape=(*[pltpu.HBM(s.shape, s.dtype) for s in shards], *[pltpu.HBM(l.shape, l.dtype) for l in lands]),
        in_specs=[HBM_SPEC] * (2 * n) + [SEM_SPEC] * (2 * n_sem) + [ANY_SPEC],
        out_specs=[HBM_SPEC] * (2 * n),
        input_output_aliases={i: i for i in range(2 * n)},
        compiler_params=pltpu.CompilerParams(has_side_effects=EFFECT),
    )(*shards, *lands, *sems, after)
    return list(out[:n]), list(out[n:])


def _gather_finish(shards, lands, name):
    n = len(shards)

    def body(*refs):
        ins, lnd = refs[:n], refs[2 * n:3 * n]
        send_sems, recv_sems, local_sems = refs[3 * n:]
        x, y, c, chips = _place()
        me_chip = 2 * x + y
        sibling = (x, y, 1 - c)

        def half(k, chip_id, pc):
            h = shards[k].shape[0] // 2
            return lnd[k].at[chip_id, pl.ds(pl.multiple_of(pc * h, 8), h), :]

        def copy(k, j, chip_id, pc):
            return pltpu.make_async_remote_copy(
                src_ref=half(k, chip_id, pc), dst_ref=half(k, chip_id, pc),
                send_sem=send_sems.at[k, j], recv_sem=recv_sems.at[k, j], device_id=sibling, device_id_type=MESH)

        locals_, started = [], []
        for k in range(n):
            cp = pltpu.make_async_copy(ins[k], lnd[k].at[me_chip], local_sems.at[k])
            cp.start()
            locals_.append(cp)
            for j, (px, py) in enumerate(chips):
                cp = copy(k, j, 2 * px + py, c)
                cp.start()
                started.append(cp)
        for k in range(n):
            for j, (px, py) in enumerate(chips):
                copy(k, j, 2 * px + py, 1 - c).wait_recv()
        for cp in started:
            cp.wait_send()
        for cp in locals_:
            cp.wait()

    out = pl.pallas_call(
        body, name=name,
        out_shape=[jax.ShapeDtypeStruct(l.shape, l.dtype) for l in lands],
        in_specs=[ANY_SPEC] * (2 * n), out_specs=[ANY_SPEC] * n,
        input_output_aliases={n + i: i for i in range(n)},
        scratch_shapes=[pltpu.SemaphoreType.DMA((n, 3)), pltpu.SemaphoreType.DMA((n, 3)),
                        pltpu.SemaphoreType.DMA((n,))],
    )(*shards, *lands)
    return list(out)


def _send_other_half(arrs, name):
    n = len(arrs)

    def body(*refs):
        ins, outs = refs[:n], refs[n:2 * n]
        send_sems, recv_sems = refs[2 * n:]
        x, y, c, _ = _place()
        cps = []
        for k in range(n):
            h = arrs[k].shape[1] // 2
            cp = pltpu.make_async_remote_copy(
                src_ref=ins[k].at[:, pl.ds(pl.multiple_of((1 - c) * h, 8), h), :], dst_ref=outs[k],
                send_sem=send_sems.at[k], recv_sem=recv_sems.at[k], device_id=(x, y, 1 - c), device_id_type=MESH)
            cp.start()
            cps.append(cp)
        for cp in cps:
            cp.wait()

    return pl.pallas_call(
        body, name=name,
        out_shape=[jax.ShapeDtypeStruct((a.shape[0], a.shape[1] // 2, a.shape[2]), a.dtype) for a in arrs],
        in_specs=[ANY_SPEC] * n, out_specs=[ANY_SPEC] * n,
        scratch_shapes=[pltpu.SemaphoreType.DMA((n,)), pltpu.SemaphoreType.DMA((n,))],
    )(*arrs)


def _blocks_to_owner(arrs, name):
    n = len(arrs)

    def body(*refs):
        ins, outs = refs[:n], refs[n:2 * n]
        send_sems, recv_sems, local_sems = refs[2 * n:]
        x, y, c, chips = _place()
        me_chip = 2 * x + y
        locals_, started = [], []
        for k in range(n):
            cp = pltpu.make_async_copy(ins[k].at[me_chip], outs[k].at[me_chip], local_sems.at[k])
            cp.start()
            locals_.append(cp)

        def copy(k, j, src_block, dst_slot, to):
            return pltpu.make_async_remote_copy(
                src_ref=ins[k].at[src_block], dst_ref=outs[k].at[dst_slot],
                send_sem=send_sems.at[k, j], recv_sem=recv_sems.at[k, j], device_id=to, device_id_type=MESH)

        for k in range(n):
            for j, (px, py) in enumerate(chips):
                cp = copy(k, j, 2 * px + py, me_chip, (px, py, c))
                cp.start()
                started.append(cp)
        for k in range(n):
            for j, (px, py) in enumerate(chips):
                copy(k, j, me_chip, 2 * px + py, (px, py, c)).wait_recv()
        for cp in started:
            cp.wait_send()
        for cp in locals_:
            cp.wait()

    return pl.pallas_call(
        body, name=name,
        out_shape=[jax.ShapeDtypeStruct(a.shape, a.dtype) for a in arrs],
        in_specs=[ANY_SPEC] * n, out_specs=[ANY_SPEC] * n,
        scratch_shapes=[pltpu.SemaphoreType.DMA((n, 3)), pltpu.SemaphoreType.DMA((n, 3)),
                        pltpu.SemaphoreType.DMA((n,))],
    )(*arrs)


def _scatter_blocks(arrs, name):
    n = len(arrs)

    def body(*refs):
        ins, outs = refs[:n], refs[n:2 * n]
        send_sems, recv_sems = refs[2 * n:]
        x, y, c, chips = _place()
        me_chip = 2 * x + y

        def copy(k, j, src_block, dst_slot, to):
            return pltpu.make_async_remote_copy(
                src_ref=ins[k].at[src_block], dst_ref=outs[k].at[dst_slot],
                send_sem=send_sems.at[k, j], recv_sem=recv_sems.at[k, j], device_id=to, device_id_type=MESH)

        started = []
        for k in range(n):
            for j, (px, py) in enumerate(chips):
                cp = copy(k, j, 2 * px + py, me_chip, (px, py, c))
                cp.start()
                started.append(cp)
        for k in range(n):
            for j, (px, py) in enumerate(chips):
                copy(k, j, me_chip, 2 * px + py, (px, py, c)).wait_recv()
        for cp in started:
            cp.wait_send()

    return pl.pallas_call(
        body, name=name,
        out_shape=[jax.ShapeDtypeStruct(a.shape, a.dtype) for a in arrs],
        in_specs=[ANY_SPEC] * n, out_specs=[ANY_SPEC] * n,
        scratch_shapes=[pltpu.SemaphoreType.DMA((n, 3)), pltpu.SemaphoreType.DMA((n, 3))],
    )(*arrs)


def _scatter_start(arrs, name):
    n = len(arrs)
    n_sem = 3 * n

    def body(*refs):
        ins, lnd = refs[:n], refs[n:2 * n]
        send_sems, recv_sems = refs[2 * n:2 * n + n_sem], refs[2 * n + n_sem:2 * n + 2 * n_sem]
        token = refs[-1]
        x, y, c, chips = _place()
        me_chip = 2 * x + y
        for k in range(n):
            for j, (px, py) in enumerate(chips):
                pltpu.make_async_remote_copy(
                    src_ref=ins[k].at[2 * px + py], dst_ref=lnd[k].at[me_chip],
                    send_sem=send_sems[3 * k + j], recv_sem=recv_sems[3 * k + j],
                    device_id=(px, py, c), device_id_type=MESH).start()
        token[...] = jnp.zeros_like(token)

    lands = [_hbm(lax.empty(a.shape, a.dtype)) for a in arrs]
    out = pl.pallas_call(
        body, name=name,
        out_shape=(*[pltpu.SemaphoreType.DMA(())] * (2 * n_sem),
                   *[pltpu.HBM(a.shape, a.dtype) for a in arrs], *[pltpu.HBM(a.shape, a.dtype) for a in arrs],
                   jax.ShapeDtypeStruct((8, 128), F32)),
        in_specs=[HBM_SPEC] * (2 * n),
        out_specs=(*[SEM_SPEC] * (2 * n_sem), *[HBM_SPEC] * (2 * n), VMEM_SPEC),
        input_output_aliases={i: 2 * n_sem + i for i in range(2 * n)},
        compiler_params=pltpu.CompilerParams(has_side_effects=EFFECT),
    )(*[_hbm(a) for a in arrs], *lands)
    base = 2 * n_sem
    return list(out[:base]), list(out[base:base + n]), list(out[base + n:base + 2 * n]), out[-1]


def _scatter_wait(sems, arrs, lands, after, name):
    n = len(arrs)
    n_sem = 3 * n

    def body(*refs):
        ins, lnd = refs[:n], refs[n:2 * n]
        s_sems, r_sems = refs[2 * n:2 * n + n_sem], refs[2 * n + n_sem:2 * n + 2 * n_sem]
        x, y, c, chips = _place()
        for k in range(n):
            for j, (px, py) in enumerate(chips):
                cp = pltpu.make_async_remote_copy(
                    src_ref=ins[k].at[2 * px + py], dst_ref=lnd[k].at[2 * px + py],
                    send_sem=s_sems[3 * k + j], recv_sem=r_sems[3 * k + j],
                    device_id=(px, py, c), device_id_type=MESH)
                cp.wait_send()
                cp.wait_recv()

    out = pl.pallas_call(
        body, name=name,
        out_shape=tuple(pltpu.HBM(a.shape, a.dtype) for a in list(arrs) + list(lands)),
        in_specs=[HBM_SPEC] * (2 * n) + [SEM_SPEC] * (2 * n_sem) + [ANY_SPEC],
        out_specs=[HBM_SPEC] * (2 * n),
        input_output_aliases={i: i for i in range(2 * n)},
        compiler_params=pltpu.CompilerParams(has_side_effects=EFFECT),
    )(*arrs, *lands, *sems, after)
    return list(out[:n]), list(out[n:])


def _sum_owner(chip_arr, pairs, got, name):
    nb, h, cols = got.shape
    tr = _row_tile(h, 16, 256)

    def body(chip_ref, own_ref, a_ref, b_ref, c_ref, o_ref):
        o_ref[...] = ((own_ref[0].astype(F32) + a_ref[0].astype(F32)) + b_ref[0].astype(F32)) + c_ref[0].astype(F32)

    def slot(off):
        return pl.BlockSpec((1, tr, cols), lambda i, chip_ref: ((chip_ref[0] + off) % N_CHIP, i, 0))

    return pl.pallas_call(
        body, name=name,
        grid_spec=pltpu.PrefetchScalarGridSpec(
            num_scalar_prefetch=1, grid=(h // tr,),
            in_specs=[slot(0), slot(1), slot(2), slot(3)],
            out_specs=pl.BlockSpec((tr, cols), lambda i, chip_ref: (i, 0))),
        out_shape=jax.ShapeDtypeStruct((h, cols), F32),
        compiler_params=_cparams(dimension_semantics=("parallel",)),
    )(chip_arr, pairs, got, got, got)


def _swap_with_sibling(arrs, name):
    n = len(arrs)

    def body(*refs):
        ins, outs = refs[:n], refs[n:2 * n]
        send_sems, recv_sems = refs[2 * n:]
        x, y, c, _ = _place()
        cps = []
        for k in range(n):
            cp = pltpu.make_async_remote_copy(
                src_ref=ins[k], dst_ref=outs[k], send_sem=send_sems.at[k], recv_sem=recv_sems.at[k],
                device_id=(x, y, 1 - c), device_id_type=MESH)
            cp.start()
            cps.append(cp)
        for cp in cps:
            cp.wait()

    return pl.pallas_call(
        body, name=name,
        out_shape=[jax.ShapeDtypeStruct(a.shape, a.dtype) for a in arrs],
        in_specs=[ANY_SPEC] * n, out_specs=[ANY_SPEC] * n,
        scratch_shapes=[pltpu.SemaphoreType.DMA((n,)), pltpu.SemaphoreType.DMA((n,))],
    )(*arrs)


def _row_tile(h, mult=8, cap=128):
    for t in range(cap - cap % mult, mult - 1, -mult):
        if h % t == 0:
            return t
    raise ValueError(h)


def _cast_bf16(a, name):
    rows, cols = a.shape
    tr = _row_tile(rows, 16, 256)

    def body(a_ref, o_ref):
        o_ref[...] = a_ref[...].astype(BF16)

    return pl.pallas_call(
        body, name=name, grid=(rows // tr,),
        out_shape=jax.ShapeDtypeStruct(a.shape, BF16),
        in_specs=[pl.BlockSpec((tr, cols), lambda i: (i, 0))],
        out_specs=pl.BlockSpec((tr, cols), lambda i: (i, 0)),
        compiler_params=_cparams(dimension_semantics=("parallel",)),
    )(a)


def _pair_sum(c_arr, full, recv, name):
    nb, rows, cols = full.shape
    h = rows // 2
    tr = _row_tile(h, 16, 256)
    steps = h // tr

    def body(c_ref, f_ref, r_ref, o_ref):
        o_ref[...] = (f_ref[...] + r_ref[...]).astype(BF16)

    return pl.pallas_call(
        body, name=name,
        grid_spec=pltpu.PrefetchScalarGridSpec(
            num_scalar_prefetch=1, grid=(nb, steps),
            in_specs=[pl.BlockSpec((1, tr, cols), lambda b, i, c_ref: (b, c_ref[0] * steps + i, 0)),
                      pl.BlockSpec((1, tr, cols), lambda b, i, c_ref: (b, i, 0))],
            out_specs=pl.BlockSpec((1, tr, cols), lambda b, i, c_ref: (b, i, 0))),
        out_shape=jax.ShapeDtypeStruct((nb, h, cols), BF16),
        compiler_params=_cparams(dimension_semantics=("parallel", "parallel")),
    )(c_arr, full, recv)


def _sum_chips(got, name):
    nb, h, cols = got.shape
    tr = _row_tile(h, 16, 256)

    def body(g_ref, o_ref):
        g = g_ref[...].astype(F32)
        o_ref[...] = ((g[0] + g[1]) + g[2]) + g[3]

    return pl.pallas_call(
        body, name=name, grid=(h // tr,),
        out_shape=jax.ShapeDtypeStruct((h, cols), F32),
        in_specs=[pl.BlockSpec((nb, tr, cols), lambda i: (0, i, 0))],
        out_specs=pl.BlockSpec((tr, cols), lambda i: (i, 0)),
        compiler_params=_cparams(dimension_semantics=("parallel",)),
    )(got)


def _adam_math(g, w, m, v):
    m1 = ADAM_B1 * m + (1.0 - ADAM_B1) * g
    v1 = ADAM_B2 * v + (1.0 - ADAM_B2) * (g * g)
    m_hat = m1 / (1.0 - ADAM_B1 ** ADAM_STEP)
    v_hat = v1 / (1.0 - ADAM_B2 ** ADAM_STEP)
    delta = -ADAM_LR * (m_hat / (jnp.sqrt(v_hat) + ADAM_EPS) + ADAM_WD * w)
    return delta, m1, v1


def _adamw_halves(c_arr, own, other, w, m, v, name):
    rows, cols = w.shape
    h = rows // 2
    tr = _row_tile(h)
    steps = h // tr

    def body(c_ref, own_ref, oth_ref, w_ref, m_ref, v_ref, g_out, d_out, m_out, v_out):
        g = jnp.where(pl.program_id(0) == c_ref[0], own_ref[...], oth_ref[...])
        d, m1, v1 = _adam_math(g, w_ref[...], m_ref[...], v_ref[...])
        g_out[...] = g
        d_out[...] = d
        m_out[...] = m1
        v_out[...] = v1

    half_spec = pl.BlockSpec((tr, cols), lambda p, i, c_ref: (i, 0))
    full_spec = pl.BlockSpec((tr, cols), lambda p, i, c_ref: (p * steps + i, 0))
    return pl.pallas_call(
        body, name=name,
        grid_spec=pltpu.PrefetchScalarGridSpec(
            num_scalar_prefetch=1, grid=(2, steps),
            in_specs=[half_spec, half_spec, full_spec, full_spec, full_spec],
            out_specs=[full_spec] * 4),
        out_shape=[jax.ShapeDtypeStruct(w.shape, F32)] * 4,
        compiler_params=_cparams(dimension_semantics=("parallel", "parallel")),
    )(c_arr, own, other, w, m, v)


def _adamw_whole(items, name):
    n = len(items)

    def body(*refs):
        ins, outs = refs[:4 * n], refs[4 * n:]
        for k in range(n):
            g, w, m, v = (r[...] for r in ins[4 * k:4 * k + 4])
            d, m1, v1 = _adam_math(g, w, m, v)
            outs[3 * k][...] = d
            outs[3 * k + 1][...] = m1
            outs[3 * k + 2][...] = v1

    flat = [a for it in items for a in it]
    shapes = [jax.ShapeDtypeStruct(it[1].shape, F32) for it in items for _ in range(3)]
    out = pl.pallas_call(
        body, name=name, out_shape=shapes,
        in_specs=[VMEM_SPEC] * (4 * n), out_specs=[VMEM_SPEC] * (3 * n),
        compiler_params=_cparams(),
    )(*flat)
    return [tuple(out[3 * k:3 * k + 3]) for k in range(n)]


def _adamw_tiled(g, w, m, v, name):
    rows, cols = w.shape
    tr = _row_tile(rows)

    def body(g_ref, w_ref, m_ref, v_ref, d_out, m_out, v_out):
        d, m1, v1 = _adam_math(g_ref[...], w_ref[...], m_ref[...], v_ref[...])
        d_out[...] = d
        m_out[...] = m1
        v_out[...] = v1

    spec = pl.BlockSpec((tr, cols), lambda i: (i, 0))
    return pl.pallas_call(
        body, name=name, grid=(rows // tr,),
        out_shape=[jax.ShapeDtypeStruct(w.shape, F32)] * 3,
        in_specs=[spec] * 4, out_specs=[spec] * 3,
        compiler_params=_cparams(dimension_semantics=("parallel",)),
    )(g, w, m, v)


def _mod_forward(cond, w_mod, b_mod_cols, name):
    def body(c_ref, w_ref, b_ref, o_ref):
        o_ref[...] = _dot(_silu(c_ref[...]), w_ref[...]) + b_ref[...]

    return pl.pallas_call(
        body, name=name, out_shape=jax.ShapeDtypeStruct((cond.shape[0], w_mod.shape[1]), F32),
        in_specs=[VMEM_SPEC] * 3, out_specs=VMEM_SPEC, compiler_params=_cparams(),
    )(cond, w_mod, b_mod_cols)


def _mod_backward(cond, w_mod, dmod_cols, name):
    def body(c_ref, w_ref, d_ref, gw_ref, gc_ref):
        s = _silu(c_ref[...])
        d = d_ref[...]
        gw_ref[...] = _dot_tn(s, d)
        gc_ref[...] = _dot_nt(d[8:16, :], w_ref[...])

    return pl.pallas_call(
        body, name=name,
        out_shape=[jax.ShapeDtypeStruct(w_mod.shape, F32), jax.ShapeDtypeStruct((8, w_mod.shape[0]), F32)],
        in_specs=[VMEM_SPEC] * 3, out_specs=[VMEM_SPEC] * 2, compiler_params=_cparams(),
    )(cond, w_mod, dmod_cols)


def _col_chunks(width, step=512):
    return [(s, min(step, width - s)) for s in range(0, width, step)]


def _in_projection(z, modc, modx, pre1, w_r, n_ctx_tiles, name):
    rows, d = z.shape
    width = w_r.shape[1]

    def body(z_ref, modc_ref, modx_ref, pre_ref, w_ref, h_ref, p_ref):
        is_ctx = pl.program_id(0) < n_ctx_tiles
        n, _ = _rms(z_ref[...])
        shift = jnp.where(is_ctx, modc_ref[0:1, :], modx_ref[0:1, :])
        scale = jnp.where(is_ctx, modc_ref[1:2, :], modx_ref[1:2, :])
        h = (n * pre_ref[...] * (1.0 + scale) + shift).astype(BF16)
        h_ref[...] = h
        for s, w in _col_chunks(width):
            p_ref[:, s:s + w] = jnp.dot(h, w_ref[:, s:s + w], preferred_element_type=F32)

    row = lambda i: (i, 0)
    fixed = lambda i: (0, 0)
    return pl.pallas_call(
        body, name=name, grid=(rows // TM,),
        out_shape=[jax.ShapeDtypeStruct((rows, d), BF16), jax.ShapeDtypeStruct((rows, width), F32)],
        in_specs=[pl.BlockSpec((TM, d), row), pl.BlockSpec((8, d), fixed), pl.BlockSpec((8, d), fixed),
                  pl.BlockSpec((1, d), fixed), VMEM_SPEC],
        out_specs=[pl.BlockSpec((TM, d), row), pl.BlockSpec((TM, width), row)],
        compiler_params=_cparams(dimension_semantics=("parallel",)),
    )(z, modc, modx, pre1, w_r)


C_HQ, C_HI, C_HF_FW, C_HF_BW, C_HGATE, C_GQ, C_GK, C_GV, C_GGATE = range(9)
OFF_GATE_HG = 9 * HW
OFF_LR = 13 * HW
P_WIDTH = OFF_LR + 128


def _head_norm_fwd(o, w):
    outs, ns, rs = [], [], []
    for h in range(NH):
        n, r = _rms(o[:, h * HD:(h + 1) * HD])
        ns.append(n)
        rs.append(r)
        outs.append(n * w)
    return jnp.concatenate(outs, axis=1), ns, rs


def _mixer_tail(z, o_hg, o_gla, p_hgate, p_ggate, p_gate_hg, p_gate_gla, hg_on, gla_on, wbh, wbg, wout):
    on_hg, n_hg, r_hg = _head_norm_fwd(o_hg, hg_on)
    on_gla, n_gla, r_gla = _head_norm_fwd(o_gla, gla_on)
    og_hg = (on_hg * _silu(p_hgate)).astype(BF16)
    og_gla = (on_gla * _silu(p_ggate)).astype(BF16)
    b_hg = jnp.dot(og_hg, wbh, preferred_element_type=F32)
    b_gla = jnp.dot(og_gla, wbg, preferred_element_type=F32)
    s_hg = _sigmoid(p_gate_hg)
    s_gla = _sigmoid(p_gate_gla)
    merged = (s_hg * b_hg + s_gla * b_gla).astype(BF16)
    y1 = jnp.dot(merged, wout, preferred_element_type=F32)
    return dict(on_hg=on_hg, n_hg=n_hg, r_hg=r_hg, on_gla=on_gla, n_gla=n_gla, r_gla=r_gla, og_hg=og_hg,
                og_gla=og_gla, b_hg=b_hg, b_gla=b_gla, s_hg=s_hg, s_gla=s_gla, merged=merged, y1=y1)


def _mixer_ffn(x_lat, p, o_list, modx, norms, onorms, w_br_hg, w_br_gla, w_out, w_gate, w_up, w_down, target,
               n_ctx_tiles, name):
    rows, d = x_lat.shape
    dff = w_gate.shape[1]
    inv_d = 1.0 / d

    def body(x_ref, ofw_hg, obw_hg, ofw_gla, obw_gla, p_hgate, p_ggate, p_ghg_a, p_ghg_b, p_ggla_a, p_ggla_b,
             modx_ref, norm_ref, on_ref, wbh_ref, wbg_ref, wout_ref, wg_ref, wu_ref, wd_ref, t_ref,
             loss_ref, dz2_ref, y1_ref, mrg_ref, oghg_ref, oggla_ref, h2_ref, a_ref, du_ref, dv_ref, dy2_ref,
             stat_ref):
        i = pl.program_id(0)
        post1, pre2, post2 = norm_ref[1:2, :], norm_ref[2:3, :], norm_ref[3:4, :]
        gate1, shift2, scale2, gate2 = modx_ref[2:3, :], modx_ref[3:4, :], modx_ref[4:5, :], modx_ref[5:6, :]
        p_gate_hg = jnp.concatenate([p_ghg_a[...], p_ghg_b[...]], axis=1)
        p_gate_gla = jnp.concatenate([p_ggla_a[...], p_ggla_b[...]], axis=1)
        t = _mixer_tail(x_ref[...], ofw_hg[...] + obw_hg[...], ofw_gla[...] + obw_gla[...], p_hgate[...],
                        p_ggate[...], p_gate_hg, p_gate_gla, on_ref[0:1, 0:HD], on_ref[1:2, 0:HD],
                        wbh_ref[...], wbg_ref[...], wout_ref[...])
        y1_ref[...] = t["y1"]
        mrg_ref[...] = t["merged"]
        oghg_ref[...] = t["og_hg"]
        oggla_ref[...] = t["og_gla"]
        n1, _ = _rms(t["y1"])
        z2 = x_ref[...] + n1 * post1 * gate1
        n2, r2 = _rms(z2)
        nw2 = n2 * pre2
        h2 = (nw2 * (1.0 + scale2) + shift2).astype(BF16)
        h2_ref[...] = h2
        u = jnp.dot(h2, wg_ref[...], preferred_element_type=F32)
        v = jnp.dot(h2, wu_ref[...], preferred_element_type=F32)
        su = _silu(u)
        a = (su * v).astype(BF16)
        a_ref[...] = a
        y2 = jnp.dot(a, wd_ref[...], preferred_element_type=F32)
        n3, r3 = _rms(y2)
        z3 = z2 + n3 * post2 * gate2
        err = z3 - t_ref[...]
        part = 0.5 * inv_d * jnp.sum(err * err)
        dz3 = err * inv_d
        dgate2 = _colsum(dz3 * n3 * post2)
        tt = dz3 * gate2
        dpost2 = _colsum(tt * n3)
        dy2 = _rms_bwd(tt * post2, n3, r3).astype(BF16)
        dy2_ref[...] = dy2
        da = _dot_nt(dy2, wd_ref[...])
        du = (da * v * _dsilu(u)).astype(BF16)
        dv = (da * su).astype(BF16)
        du_ref[...] = du
        dv_ref[...] = dv
        dh2 = _dot_nt(du, wg_ref[...]) + _dot_nt(dv, wu_ref[...])
        dshift2 = _colsum(dh2)
        dscale2 = _colsum(dh2 * nw2)
        dnw2 = dh2 * (1.0 + scale2)
        dpre2 = _colsum(dnw2 * n2)
        dz2_ref[...] = dz3 + _rms_bwd(dnw2 * pre2, n2, r2)

        @pl.when(i == 0)
        def _():
            stat_ref[...] = jnp.zeros_like(stat_ref)
            loss_ref[...] = jnp.zeros_like(loss_ref)

        for r, val in enumerate((dshift2, dscale2, dgate2, dpre2, dpost2)):
            stat_ref[r:r + 1, :] += val
        loss_ref[...] += part

    tm = TM_FFN
    ctx_tiles = n_ctx_tiles * (TM // tm)
    lat = lambda i: (i, 0)
    full = lambda i: (i + ctx_tiles, 0)
    fixed = lambda i: (0, 0)

    def pcol(blk):
        return pl.BlockSpec((tm, HW), lambda i: (i + ctx_tiles, blk))

    in_specs = ([pl.BlockSpec((tm, d), lat)] + [pl.BlockSpec((tm, HW), full)] * 4
                + [pcol(C_HGATE), pcol(C_GGATE), pcol(9), pcol(10), pcol(11), pcol(12)]
                + [pl.BlockSpec((8, d), fixed), pl.BlockSpec((8, d), fixed), pl.BlockSpec((8, d), fixed)]
                + [VMEM_SPEC] * 6 + [pl.BlockSpec((tm, d), lat)])
    bf = lambda w: jax.ShapeDtypeStruct((rows, w), BF16)
    out_shape = [jax.ShapeDtypeStruct((8, 128), F32), jax.ShapeDtypeStruct((rows, d), F32),
                 jax.ShapeDtypeStruct((rows, d), F32), bf(d), bf(HW), bf(HW), bf(d), bf(dff), bf(dff), bf(dff), bf(d),
                 jax.ShapeDtypeStruct((8, d), F32)]
    out_specs = [pl.BlockSpec((8, 128), fixed), pl.BlockSpec((tm, d), lat), pl.BlockSpec((tm, d), lat),
                 pl.BlockSpec((tm, d), lat), pl.BlockSpec((tm, HW), lat), pl.BlockSpec((tm, HW), lat),
                 pl.BlockSpec((tm, d), lat), pl.BlockSpec((tm, dff), lat), pl.BlockSpec((tm, dff), lat),
                 pl.BlockSpec((tm, dff), lat), pl.BlockSpec((tm, d), lat), pl.BlockSpec((8, d), fixed)]
    return pl.pallas_call(
        body, name=name, grid=(rows // tm,), out_shape=out_shape, in_specs=in_specs, out_specs=out_specs,
        compiler_params=_cparams(dimension_semantics=("arbitrary",)),
    )(x_lat, *o_list, p, p, p, p, p, p, modx, norms, onorms, w_br_hg, w_br_gla, w_out, w_gate, w_up, w_down, target)


def _mixer_tail_bwd(x_lat, p, o_list, dz2, y1, modx, norms, onorms, w_br_hg, w_br_gla, w_out, n_ctx_tiles, n_tiles,
                    name):
    rows, d = x_lat.shape
    total = n_tiles * TM

    def body(x_ref, ofw_hg, obw_hg, ofw_gla, obw_gla, p_hgate, p_ggate, p_ghg_a, p_ghg_b, p_ggla_a, p_ggla_b,
             dz2_ref, y1_ref, modx_ref, norm_ref, on_ref, wbh_ref, wbg_ref, wout_ref,
             dohg_ref, dogla_ref, dhgate_ref, dggate_ref, dghg_ref, dggla_ref, dy1_ref, dbhg_ref, dbgla_ref,
             stat_ref):
        i = pl.program_id(0)

        @pl.when(i == 0)
        def _():
            stat_ref[...] = jnp.zeros_like(stat_ref)

        @pl.when(i < n_ctx_tiles)
        def _():
            for ref in (dohg_ref, dogla_ref, dhgate_ref, dggate_ref, dghg_ref, dggla_ref):
                ref[...] = jnp.zeros_like(ref)

        @pl.when(i >= n_ctx_tiles)
        def _():
            post1, gate1 = norm_ref[1:2, :], modx_ref[2:3, :]
            hg_on, gla_on = on_ref[0:1, 0:HD], on_ref[1:2, 0:HD]
            p_gate_hg = jnp.concatenate([p_ghg_a[...], p_ghg_b[...]], axis=1)
            p_gate_gla = jnp.concatenate([p_ggla_a[...], p_ggla_b[...]], axis=1)
            ph, pg = p_hgate[...], p_ggate[...]
            t = _mixer_tail(x_ref[...], ofw_hg[...] + obw_hg[...], ofw_gla[...] + obw_gla[...], ph, pg,
                            p_gate_hg, p_gate_gla, hg_on, gla_on, wbh_ref[...], wbg_ref[...], wout_ref[...])
            dz2 = dz2_ref[...]
            n1, r1 = _rms(y1_ref[...])
            dgate1 = _colsum(dz2 * n1 * post1)
            tt = dz2 * gate1
            dpost1 = _colsum(tt * n1)
            dy1 = _rms_bwd(tt * post1, n1, r1).astype(BF16)
            dy1_ref[...] = dy1
            dmerged = _dot_nt(dy1, wout_ref[...])
            dghg_ref[...] = dmerged * t["b_hg"] * t["s_hg"] * (1.0 - t["s_hg"])
            dggla_ref[...] = dmerged * t["b_gla"] * t["s_gla"] * (1.0 - t["s_gla"])
            db_hg = (dmerged * t["s_hg"]).astype(BF16)
            db_gla = (dmerged * t["s_gla"]).astype(BF16)
            dbhg_ref[...] = db_hg
            dbgla_ref[...] = db_gla
            don_acc = []
            for (db, wb, pgate, on, ns, rs, gain, gate_ref, do_ref) in (
                    (db_hg, wbh_ref, ph, t["on_hg"], t["n_hg"], t["r_hg"], hg_on, dhgate_ref, dohg_ref),
                    (db_gla, wbg_ref, pg, t["on_gla"], t["n_gla"], t["r_gla"], gla_on, dggate_ref, dogla_ref)):
                dog = _dot_nt(db, wb[...])
                gate_ref[...] = dog * on * _dsilu(pgate)
                don = dog * _silu(pgate)
                acc = jnp.zeros((1, HD), F32)
                for h in range(NH):
                    sl = slice(h * HD, (h + 1) * HD)
                    acc = acc + _colsum(don[:, sl] * ns[h])
                    do_ref[:, sl] = _rms_bwd(don[:, sl] * gain, ns[h], rs[h])
                don_acc.append(acc)
            stat_ref[0:1, :] += dgate1
            stat_ref[1:2, :] += dpost1
            stat_ref[2:3, 0:HD] += don_acc[0]
            stat_ref[2:3, HD:2 * HD] += don_acc[1]

    lat = lambda i: (jnp.maximum(i - n_ctx_tiles, 0), 0)
    full = lambda i: (i, 0)
    fixed = lambda i: (0, 0)

    def pcol(blk):
        return pl.BlockSpec((TM, HW), lambda i: (i, blk))

    in_specs = ([pl.BlockSpec((TM, d), lat)] + [pl.BlockSpec((TM, HW), full)] * 4
                + [pcol(C_HGATE), pcol(C_GGATE), pcol(9), pcol(10), pcol(11), pcol(12)]
                + [pl.BlockSpec((TM, d), lat), pl.BlockSpec((TM, d), lat)]
                + [pl.BlockSpec((8, d), fixed)] * 3 + [VMEM_SPEC] * 3)
    f = lambda w: jax.ShapeDtypeStruct((total, w), F32)
    out_shape = [f(HW), f(HW), f(HW), f(HW), f(d), f(d), jax.ShapeDtypeStruct((rows, d), BF16),
                 jax.ShapeDtypeStruct((rows, d), BF16), jax.ShapeDtypeStruct((rows, d), BF16),
                 jax.ShapeDtypeStruct((8, d), F32)]
    out_specs = ([pl.BlockSpec((TM, HW), full)] * 4 + [pl.BlockSpec((TM, d), full)] * 2
                 + [pl.BlockSpec((TM, d), lat)] * 3 + [pl.BlockSpec((8, d), fixed)])
    return pl.pallas_call(
        body, name=name, grid=(n_tiles,), out_shape=out_shape, in_specs=in_specs, out_specs=out_specs,
        compiler_params=_cparams(dimension_semantics=("arbitrary",)),
    )(x_lat, *o_list, p, p, p, p, p, p, dz2, y1, modx, norms, onorms, w_br_hg, w_br_gla, w_out)


def _in_projection_bwd(z, dz2, modc, modx, pre1, w_r, pieces, n_ctx_tiles, name):
    rows, d = z.shape
    lat_rows = dz2.shape[0]
    width = w_r.shape[1]
    n_pieces = len(pieces)

    def body(*refs):
        z_ref, dz2_ref, modc_ref, modx_ref, pre_ref, w_ref = refs[:6]
        (dhq_f, dhq_b, dhi_f, dhi_b, dhf_f, dhf_b, dhgate, dgq_f, dgq_b, dgk_f, dgk_b, dgv_f, dgv_b, dggate,
         dghg, dggla, dlr_f, dlr_b) = refs[6:6 + n_pieces]
        dp_ref, gx_ref, stat_ref = refs[6 + n_pieces:]
        i = pl.program_id(0)
        is_ctx = i < n_ctx_tiles
        sections = [
            (0, dhq_f[...] + dhq_b[...]), (HW, dhi_f[...] + dhi_b[...]), (2 * HW, dhf_f[...]), (3 * HW, dhf_b[...]),
            (4 * HW, dhgate[...]), (5 * HW, dgq_f[...] + dgq_b[...]), (6 * HW, dgk_f[...] + dgk_b[...]),
            (7 * HW, dgv_f[...] + dgv_b[...]), (8 * HW, dggate[...]),
            (9 * HW, dghg[:, 0:HW]), (10 * HW, dghg[:, HW:2 * HW]),
            (11 * HW, dggla[:, 0:HW]), (12 * HW, dggla[:, HW:2 * HW]), (OFF_LR, dlr_f[...] + dlr_b[...])]
        dh = jnp.zeros((TM, d), F32)
        for off, val in sections:
            w = val.shape[1]
            vb = val.astype(BF16)
            dp_ref[:, off:off + w] = vb
            dh = dh + _dot_nt(vb, w_ref[:, off:off + w])
        n, r = _rms(z_ref[...])
        pre = pre_ref[...]
        scale = jnp.where(is_ctx, modc_ref[1:2, :], modx_ref[1:2, :])
        nw = n * pre
        dshift = _colsum(dh)
        dscale = _colsum(dh * nw)
        dnw = dh * (1.0 + scale)
        dpre = _colsum(dnw * n)
        gx_ref[...] = dz2_ref[...] + _rms_bwd(dnw * pre, n, r)
        zero = jnp.zeros((1, d), F32)

        @pl.when(i == 0)
        def _():
            stat_ref[...] = jnp.zeros_like(stat_ref)

        stat_ref[0:1, :] += jnp.where(is_ctx, zero, dshift)
        stat_ref[1:2, :] += jnp.where(is_ctx, zero, dscale)
        stat_ref[2:3, :] += jnp.where(is_ctx, dshift, zero)
        stat_ref[3:4, :] += jnp.where(is_ctx, dscale, zero)
        stat_ref[4:5, :] += dpre

    full = lambda i: (i, 0)
    lat = lambda i: (jnp.maximum(i - n_ctx_tiles, 0), 0)
    fixed = lambda i: (0, 0)
    piece_specs = [pl.BlockSpec((TM, a.shape[1]), full) for a in pieces]
    in_specs = [pl.BlockSpec((TM, d), full), pl.BlockSpec((TM, d), lat), pl.BlockSpec((8, d), fixed),
                pl.BlockSpec((8, d), fixed), pl.BlockSpec((1, d), fixed), VMEM_SPEC] + piece_specs
    return pl.pallas_call(
        body, name=name, grid=(rows // TM,),
        out_shape=[jax.ShapeDtypeStruct((rows, width), BF16), jax.ShapeDtypeStruct((lat_rows, d), F32),
                   jax.ShapeDtypeStruct((8, d), F32)],
        in_specs=in_specs,
        out_specs=[pl.BlockSpec((TM, width), full), pl.BlockSpec((TM, d), lat), pl.BlockSpec((8, d), fixed)],
        compiler_params=_cparams(dimension_semantics=("arbitrary",)),
    )(z, dz2, modc, modx, pre1, w_r, *pieces)


def _weight_grad(xs, dy, name, col_block=None, tk=256, tn=512):
    rows = dy.shape[0]
    k = xs.shape[1]
    if col_block is None:
        n, tn_, cb = dy.shape[1], tn, 0
    else:
        n, tn_, cb = col_block[0], col_block[0], col_block[1]
    tn_ = min(tn_, n)
    tk_ = min(tk, k)

    def body(x_ref, dy_ref, o_ref):
        o_ref[...] = _dot_tn(x_ref[...], dy_ref[...])

    return pl.pallas_call(
        body, name=name, grid=(k // tk_, n // tn_),
        out_shape=jax.ShapeDtypeStruct((k, n), F32),
        in_specs=[pl.BlockSpec((rows, tk_), lambda i, j: (0, i)),
                  pl.BlockSpec((rows, tn_), lambda i, j: (0, j + cb))],
        out_specs=pl.BlockSpec((tk_, tn_), lambda i, j: (i, j)),
        compiler_params=_cparams(dimension_semantics=("parallel", "parallel")),
    )(xs, dy)


def _chunk_terms(q, k, g, fw):
    c = CHUNK
    r = lax.broadcasted_iota(jnp.int32, (c, c), 0)
    s = lax.broadcasted_iota(jnp.int32, (c, c), 1)
    causal = (s <= r) if fw else (s >= r)
    causal_t = (s >= r) if fw else (s <= r)
    cum = jnp.dot(causal.astype(F32), g, precision=HIGHEST, preferred_element_type=F32)
    row = lax.broadcasted_iota(jnp.int32, (c, 1), 0)
    pos = row if fw else (c - 1 - row)
    starts = [None]
    for j in range(1, NSUB):
        rj = SUB * j - 1 if fw else c - SUB * j
        starts.append(cum[rj:rj + 1, :])
    in_blk = [(pos >= SUB * j) & (pos < SUB * (j + 1)) for j in range(NSUB)]
    e = [jnp.exp(cum)]
    for j in range(1, NSUB):
        e.append(jnp.exp(jnp.where(pos >= SUB * j, cum - starts[j], -1e30)))
    own = jnp.zeros_like(cum)
    for j in range(1, NSUB):
        own = own + jnp.where(in_blk[j], starts[j], 0.0)
    kscale = jnp.exp(own - cum)
    rend = c - 1 if fw else 0
    cend = cum[rend:rend + 1, :]
    tail = jnp.exp(cend - cum)
    qcat = jnp.concatenate([q * e[j] for j in range(NSUB)], axis=1).astype(BF16)
    kt = k * kscale
    km = jnp.concatenate([jnp.where(in_blk[j], kt, 0.0) for j in range(NSUB)], axis=1).astype(BF16)
    return dict(causal=causal, causal_t=causal_t, e=e, in_blk=in_blk, kscale=kscale, cend=cend, tail=tail,
                qcat=qcat, km=km)


def _chunk_fwd(q, k, v, g, st0, fw):
    t = _chunk_terms(q, k, g, fw)
    a = jnp.where(t["causal"], _dot_nt(t["qcat"], t["km"]), 0.0)
    o = _dot(a, v) + _dot_nt(t["qcat"][:, 0:HD], st0)
    st1 = st0 * jnp.exp(t["cend"]) + _dot_tn(v, k * t["tail"])
    return o, st1


def _chunk_bwd(q, k, v, g, st0, do, dst1, fw):
    t = _chunk_terms(q, k, g, fw)
    qcat, km, e = t["qcat"], t["km"], t["e"]
    a_t = jnp.where(t["causal_t"], _dot_nt(km, qcat), 0.0)
    ktail = k * t["tail"]
    dv = _dot(a_t, do) + _dot_nt(ktail, dst1)
    da = jnp.where(t["causal"], _dot_nt(do, v), 0.0)
    da_t = jnp.where(t["causal_t"], _dot_nt(v, do), 0.0)
    dqcat = _dot(da, km)
    dq_inter = e[0] * _dot(do, st0)
    dq = dq_inter
    for j in range(NSUB):
        dq = dq + e[j] * dqcat[:, j * HD:(j + 1) * HD]
    dkm = _dot(da_t, qcat)
    dkt = jnp.zeros_like(k)
    for j in range(NSUB):
        dkt = dkt + jnp.where(t["in_blk"][j], dkm[:, j * HD:(j + 1) * HD], 0.0)
    dk_inter = _dot(v, dst1) * t["tail"]
    dk = dkt * t["kscale"] + dk_inter
    dcum = q * dq_inter - k * dk_inter
    for j in range(NSUB):
        sl = slice(j * HD, (j + 1) * HD)
        dcum = dcum + qcat[:, sl].astype(F32) * dqcat[:, sl] - km[:, sl].astype(F32) * dkm[:, sl]
    ecend = jnp.exp(t["cend"])
    end = ecend * _colsum(st0 * dst1) + _colsum(k * dk_inter)
    dg = jnp.dot(t["causal_t"].astype(F32), dcum, precision=HIGHEST, preferred_element_type=F32) + end
    dst0 = dst1 * ecend + _dot_tn(do, q * e[0])
    return dq, dk, dv, dg, dst0


def _chunk_index(step, n_ctx_chunks, n_chunks, fw):
    if fw:
        return step
    return jnp.where(step < n_ctx_chunks, n_ctx_chunks - 1 - step, n_chunks - 1 + n_ctx_chunks - step)


def _hg_inputs(hq, hf, lbv, d_idx, sl):
    lb = _sigmoid(lbv[d_idx:d_idx + 1, sl] - lbv[2 + d_idx:3 + d_idx, sl])
    sg = _sigmoid(hf)
    f = lb + (1.0 - lb) * sg
    return _silu(hq), 1.0 - f, jnp.log(f), f, sg, lb


def _scan_fwd(p, side, n_ctx_chunks, fw, branch, name):
    rows = p.shape[0]
    n_chunks = rows // CHUNK
    d_idx = 0 if fw else 1
    hg = branch == "hg"
    cols = (C_HQ, C_HI, C_HF_FW + d_idx) if hg else (C_GQ, C_GK, C_GV)

    def body(*refs):
        if hg:
            a_ref, b_ref, c_ref, lb_ref, o_ref, st_ref, state = refs
        else:
            a_ref, b_ref, c_ref, lr_ref, wgk_ref, bgk_ref, o_ref, st_ref, state = refs
            logits = _dot(lr_ref[...], wgk_ref[...]) + bgk_ref[...]
            g_all = _log_sigmoid(logits) * (1.0 / GATE_NORM)

        @pl.when(pl.program_id(0) == 0)
        def _():
            state[...] = jnp.zeros_like(state)

        for h in range(NH):
            sl = slice(h * HD, (h + 1) * HD)
            if hg:
                q, k, g, _, _, _ = _hg_inputs(a_ref[:, sl], c_ref[:, sl], lb_ref[...], d_idx, sl)
                v = b_ref[:, sl]
            else:
                q, k, v, g = a_ref[:, sl] * (HD ** -0.5), b_ref[:, sl], c_ref[:, sl], g_all[:, sl]
            st0 = state[h]
            st_ref[0, h] = st0
            o, st1 = _chunk_fwd(q, k, v, g, st0, fw)
            o_ref[:, sl] = o
            state[h] = st1

    def cmap(blk):
        return pl.BlockSpec((CHUNK, HW), lambda j: (_chunk_index(j, n_ctx_chunks, n_chunks, fw), blk))

    fixed = lambda j: (0, 0)
    in_specs = [cmap(cols[0]), cmap(cols[1]), cmap(cols[2])]
    if hg:
        in_specs += [pl.BlockSpec((4, HW), fixed)]
        args = (p, p, p, side)
    else:
        in_specs += [pl.BlockSpec((CHUNK, 128), lambda j: (_chunk_index(j, n_ctx_chunks, n_chunks, fw), OFF_LR // 128)),
                     pl.BlockSpec((128, HW), fixed), pl.BlockSpec((1, HW), fixed)]
        args = (p, p, p, p, side[0], side[1])
    return pl.pallas_call(
        body, name=name, grid=(n_chunks,),
        out_shape=[jax.ShapeDtypeStruct((rows, HW), F32), jax.ShapeDtypeStruct((n_chunks, NH, HD, HD), F32)],
        in_specs=in_specs,
        out_specs=[pl.BlockSpec((CHUNK, HW), lambda j: (_chunk_index(j, n_ctx_chunks, n_chunks, fw), 0)),
                   pl.BlockSpec((1, NH, HD, HD), lambda j: (_chunk_index(j, n_ctx_chunks, n_chunks, fw), 0, 0, 0))],
        scratch_shapes=[pltpu.VMEM((NH, HD, HD), F32)],
        compiler_params=_cparams(dimension_semantics=("arbitrary",)),
    )(*args)


def _scan_bwd(p, side, states, d_o, n_ctx_chunks, fw, branch, name):
    rows = p.shape[0]
    n_chunks = rows // CHUNK
    d_idx = 0 if fw else 1
    hg = branch == "hg"
    cols = (C_HQ, C_HI, C_HF_FW + d_idx) if hg else (C_GQ, C_GK, C_GV)

    def body(*refs):
        if hg:
            a_ref, b_ref, c_ref, lb_ref, st_ref, do_ref, da_ref, db_ref, dc_ref, dlb_ref, dstate = refs
        else:
            (a_ref, b_ref, c_ref, lr_ref, wgk_ref, bgk_ref, st_ref, do_ref, da_ref, db_ref, dc_ref, dlr_ref,
             dwgk_ref, dbias_ref, dstate) = refs
            lr = lr_ref[...]
            logits = _dot(lr, wgk_ref[...]) + bgk_ref[...]
            g_all = _log_sigmoid(logits) * (1.0 / GATE_NORM)

        @pl.when(pl.program_id(0) == 0)
        def _():
            dstate[...] = jnp.zeros_like(dstate)
            if hg:
                dlb_ref[...] = jnp.zeros_like(dlb_ref)
            else:
                dwgk_ref[...] = jnp.zeros_like(dwgk_ref)
                dbias_ref[...] = jnp.zeros_like(dbias_ref)

        dg_parts = []
        for h in range(NH):
            sl = slice(h * HD, (h + 1) * HD)
            if hg:
                hq, hf = a_ref[:, sl], c_ref[:, sl]
                q, k, g, f, sg, lb = _hg_inputs(hq, hf, lb_ref[...], d_idx, sl)
                v = b_ref[:, sl]
            else:
                q, k, v, g = a_ref[:, sl] * (HD ** -0.5), b_ref[:, sl], c_ref[:, sl], g_all[:, sl]
            dq, dk, dv, dg, dst0 = _chunk_bwd(q, k, v, g, st_ref[0, h], do_ref[:, sl], dstate[h], fw)
            dstate[h] = dst0
            if hg:
                da_ref[:, sl] = dq * _dsilu(hq)
                db_ref[:, sl] = dv
                df = dg / f - dk
                dc_ref[:, sl] = df * (1.0 - lb) * sg * (1.0 - sg)
                dlb_ref[0:1, sl] += _colsum(df * (1.0 - sg))
            else:
                da_ref[:, sl] = dq * (HD ** -0.5)
                db_ref[:, sl] = dk
                dc_ref[:, sl] = dv
                dg_parts.append(dg)
        if not hg:
            dlogits = jnp.concatenate(dg_parts, axis=1) * (1.0 / GATE_NORM) * (1.0 - _sigmoid(logits))
            dlr_ref[...] = _dot_nt(dlogits, wgk_ref[...])
            dwgk_ref[...] += _dot_tn(lr, dlogits)
            dbias_ref[0:1, :] += _colsum(dlogits)

    def chunk_of(j):
        return _chunk_index(n_chunks - 1 - j, n_ctx_chunks, n_chunks, fw)

    def cmap(blk, width=HW):
        return pl.BlockSpec((CHUNK, width), lambda j: (chunk_of(j), blk))

    fixed = lambda j: (0, 0)
    st_spec = pl.BlockSpec((1, NH, HD, HD), lambda j: (chunk_of(j), 0, 0, 0))
    big = jax.ShapeDtypeStruct((rows, HW), F32)
    if hg:
        in_specs = [cmap(cols[0]), cmap(cols[1]), cmap(cols[2]), pl.BlockSpec((4, HW), fixed), st_spec, cmap(0)]
        args = (p, p, p, side, states, d_o)
        out_shape = [big, big, big, jax.ShapeDtypeStruct((8, HW), F32)]
        out_specs = [cmap(0), cmap(0), cmap(0), pl.BlockSpec((8, HW), fixed)]
    else:
        in_specs = [cmap(cols[0]), cmap(cols[1]), cmap(cols[2]), cmap(OFF_LR // 128, 128),
                    pl.BlockSpec((128, HW), fixed), pl.BlockSpec((1, HW), fixed), st_spec, cmap(0)]
        args = (p, p, p, p, side[0], side[1], states, d_o)
        out_shape = [big, big, big, jax.ShapeDtypeStruct((rows, 128), F32), jax.ShapeDtypeStruct((128, HW), F32),
                     jax.ShapeDtypeStruct((8, HW), F32)]
        out_specs = [cmap(0), cmap(0), cmap(0), cmap(0, 128), pl.BlockSpec((128, HW), fixed),
                     pl.BlockSpec((8, HW), fixed)]
    return pl.pallas_call(
        body, name=name, grid=(n_chunks,), out_shape=out_shape, in_specs=in_specs, out_specs=out_specs,
        scratch_shapes=[pltpu.VMEM((NH, HD, HD), F32)],
        compiler_params=_cparams(dimension_semantics=("arbitrary",)),
    )(*args)


SMALL_ROWS = 56
ROWS_MOD_X = (0, 1, 8, 16, 17, 18)
ROWS_MOD_C = (2, 3)
ROW_PRE1, ROW_POST1, ROW_ONORM, ROW_PRE2, ROW_POST2, ROW_LB, ROW_BGK, ROW_WGK = 4, 9, 10, 19, 20, 24, 32, 40


def _reduce_small(gathered, lb_full, name):
    _, _, d = gathered.shape

    def body(g_ref, lb_ref, sum_ref, dmod_ref, dbmod_ref, dlb_ref):
        total = g_ref[0]
        for b in range(1, N_DEV):
            total = total + g_ref[b]
        sum_ref[...] = total
        dmod_ref[...] = jnp.zeros_like(dmod_ref)
        for m in range(N_MOD):
            col = slice(m * d, (m + 1) * d)
            acc = jnp.zeros((1, d), F32)
            for b in range(N_DEV):
                row = g_ref[b, ROWS_MOD_X[m]:ROWS_MOD_X[m] + 1, :]
                dmod_ref[b:b + 1, col] = row
                acc = acc + row
            if m < 2:
                ctx_row = total[ROWS_MOD_C[m]:ROWS_MOD_C[m] + 1, :]
                dmod_ref[8:9, col] = ctx_row
                acc = acc + ctx_row
            dbmod_ref[:, col] = acc
        lbv = lb_ref[...]
        for dd in range(2):
            lb = _sigmoid(lbv[dd:dd + 1, :] - lbv[2 + dd:3 + dd, :])
            gl = total[ROW_LB:ROW_LB + 1, dd * HW:(dd + 1) * HW] * lb * (1.0 - lb)
            dlb_ref[dd:dd + 1, :] = gl
            dlb_ref[2 + dd:3 + dd, :] = -gl

    return pl.pallas_call(
        body, name=name,
        out_shape=[jax.ShapeDtypeStruct((SMALL_ROWS, d), F32), jax.ShapeDtypeStruct((16, N_MOD * d), F32),
                   jax.ShapeDtypeStruct((1, N_MOD * d), F32), jax.ShapeDtypeStruct((4, HW), F32)],
        in_specs=[VMEM_SPEC] * 2, out_specs=[VMEM_SPEC] * 4, compiler_params=_cparams(),
    )(gathered, lb_full)


def _c_ctx_grad(gathered, c_ctx_row, name):
    def body(g_ref, c_ref, o_ref):
        acc = g_ref[0, 0:1, :]
        for chip in range(1, N_CHIP):
            acc = acc + g_ref[2 * chip, 0:1, :]
        o_ref[...] = acc * _dsilu(c_ref[...])

    return pl.pallas_call(
        body, name=name, out_shape=jax.ShapeDtypeStruct(c_ctx_row.shape, F32),
        in_specs=[VMEM_SPEC] * 2, out_specs=VMEM_SPEC, compiler_params=_cparams(),
    )(gathered, c_ctx_row)


def _relayout_w_in(w):
    pad = jnp.zeros((w.shape[0], 128 - 2 * RANK), w.dtype)
    return jnp.concatenate([w[:, :9 * HW], w[:, 9 * HW + 2 * RANK:], w[:, 9 * HW:9 * HW + 2 * RANK], pad], axis=1)


def _unrelayout_w_in(g_main, g_lr):
    return jnp.concatenate([g_main[:, :9 * HW], g_lr[:, :2 * RANK], g_main[:, 9 * HW:]], axis=1)


def _blocked(full, n_blocks):
    k, n = full.shape
    return full.reshape(k, n_blocks, n // n_blocks).transpose(1, 0, 2)


def _unblocked(blocks):
    nb, k, n = blocks.shape
    return blocks.transpose(1, 0, 2).reshape(k, nb * n)


def _sample_front(x0, ctx0, modc, modx, norm_pre1, lb_full, gla_side, w_in_r):
    ctx_len = ctx0.shape[0]
    n_ctx_tiles = ctx_len // TM
    n_ctx_chunks = ctx_len // CHUNK
    z = jnp.concatenate([ctx0, x0], axis=0)
    h1, p = _in_projection(z, modc, modx, norm_pre1, w_in_r, n_ctx_tiles, "in_projection")
    o_hg_fw, st_hg_fw = _scan_fwd(p, lb_full, n_ctx_chunks, True, "hg", "scan_hg_fw")
    o_hg_bw, st_hg_bw = _scan_fwd(p, lb_full, n_ctx_chunks, False, "hg", "scan_hg_bw")
    o_gla_fw, st_gla_fw = _scan_fwd(p, gla_side[0], n_ctx_chunks, True, "gla", "scan_gla_fw")
    o_gla_bw, st_gla_bw = _scan_fwd(p, gla_side[1], n_ctx_chunks, False, "gla", "scan_gla_bw")
    return dict(z=z, h1=h1, p=p, o_list=[o_hg_fw, o_hg_bw, o_gla_fw, o_gla_bw],
                states=[st_hg_fw, st_hg_bw, st_gla_fw, st_gla_bw])


def _sample_back(reduce, front, x0, ctx0, target0, modc, modx, norm_pre1, norms, onorms, lb_full, gla_side, w_in_r,
                 wbh, wbg, wout, wg, wu, wd):
    seq, d = x0.shape
    ctx_len = ctx0.shape[0]
    n_ctx_tiles = ctx_len // TM
    n_tiles = (ctx_len + seq) // TM
    n_ctx_chunks = ctx_len // CHUNK
    z, h1, p, o_list = front["z"], front["h1"], front["p"], front["o_list"]
    st_hg_fw, st_hg_bw, st_gla_fw, st_gla_bw = front["states"]
    (loss_part, dz2, y1, merged, og_hg, og_gla, h2, a_act, du, dv, dy2, stat_ffn) = _mixer_ffn(
        x0, p, o_list, modx, norms, onorms, wbh, wbg, wout, wg, wu, wd, target0, n_ctx_tiles, "mixer_ffn")
    dff = wg.shape[1]
    tn_ff = dff // N_CHIP if (dff // N_CHIP) % 128 == 0 else 256
    tok = reduce("ffn", [_weight_grad(h2, du, "grad_w_ff_gate", tn=tn_ff),
                         _weight_grad(h2, dv, "grad_w_ff_up", tn=tn_ff),
                         _weight_grad(a_act, dy2, "grad_w_ff_down", tk=tn_ff)])

    (d_ohg, d_ogla, d_hgate, d_ggate, d_ghg, d_ggla, dy1, db_hg, db_gla, stat_mix) = _mixer_tail_bwd(
        x0, p, o_list, dz2, y1, modx + tok, norms, onorms, wbh, wbg, wout, n_ctx_tiles, n_tiles, "mixer_tail_bwd")
    tok = reduce("mix", [_weight_grad(og_hg, db_hg, "grad_w_br_hg"), _weight_grad(og_gla, db_gla, "grad_w_br_gla"),
                         _weight_grad(merged, dy1, "grad_w_out")])
    lb_b = lb_full + tok
    gla_b = [(wgk, bias + tok) for wgk, bias in gla_side]
    dhq_f, dhi_f, dhf_f, dlb_f = _scan_bwd(p, lb_b, st_hg_fw, d_ohg, n_ctx_chunks, True, "hg", "scan_hg_fw_bwd")
    dhq_b, dhi_b, dhf_b, dlb_b = _scan_bwd(p, lb_b, st_hg_bw, d_ohg, n_ctx_chunks, False, "hg", "scan_hg_bw_bwd")
    dgq_f, dgk_f, dgv_f, dlr_f, dwgk_f, dbgk_f = _scan_bwd(p, gla_b[0], st_gla_fw, d_ogla, n_ctx_chunks, True, "gla",
                                                           "scan_gla_fw_bwd")
    dgq_b, dgk_b, dgv_b, dlr_b, dwgk_b, dbgk_b = _scan_bwd(p, gla_b[1], st_gla_bw, d_ogla, n_ctx_chunks, False, "gla",
                                                           "scan_gla_bw_bwd")
    pieces = [dhq_f, dhq_b, dhi_f, dhi_b, dhf_f, dhf_b, d_hgate, dgq_f, dgq_b, dgk_f, dgk_b, dgv_f, dgv_b, d_ggate,
              d_ghg, d_ggla, dlr_f, dlr_b]
    dp, grad_x, stat_in = _in_projection_bwd(z, dz2, modc, modx, norm_pre1, w_in_r, pieces, n_ctx_tiles,
                                             "in_projection_bwd")

    g_in_main = _weight_grad_cols(h1, dp, OFF_LR, "grad_w_in_main")
    g_in_lr = _weight_grad(h1, dp, "grad_w_in_lr", col_block=(128, OFF_LR // 128))
    reduce("in", [_unrelayout_w_in(g_in_main, g_in_lr)])
    return dict(
        loss_part=loss_part, grad_x=grad_x, stat_in=stat_in, stat_mix=stat_mix, stat_ffn=stat_ffn,
        dlb=(dlb_f, dlb_b), dwgk=(dwgk_f, dwgk_b), dbgk=(dbgk_f, dbgk_b))


def kernel(x, c, ctx, c_ctx, w_mod, b_mod, norm_pre1, norm_post1, norm_pre2, norm_post2, w_in, hg_lb, hg_onorm, gla_w_gk, gla_b_gk, gla_onorm, w_br_hg, w_br_gla, w_out, w_ff_gate, w_ff_up, w_ff_down, loss_target, m_c_ctx, m_w_mod, m_b_mod, m_norm_pre1, m_norm_post1, m_norm_pre2, m_norm_post2, m_w_in, m_hg_lb, m_hg_onorm, m_gla_w_gk, m_gla_b_gk, m_gla_onorm, m_w_br_hg, m_w_br_gla, m_w_out, m_w_ff_gate, m_w_ff_up, m_w_ff_down, v_c_ctx, v_w_mod, v_b_mod, v_norm_pre1, v_norm_post1, v_norm_pre2, v_norm_post2, v_w_in, v_hg_lb, v_hg_onorm, v_gla_w_gk, v_gla_b_gk, v_gla_onorm, v_w_br_hg, v_w_br_gla, v_w_out, v_w_ff_gate, v_w_ff_up, v_w_ff_down):
    seq, d = x.shape[1], x.shape[2]
    ctx_len = ctx.shape[1]
    assert seq % TM == 0 and ctx_len % TM == 0 and d == 2 * HW
    ax, ay, ac = lax.axis_index("x"), lax.axis_index("y"), lax.axis_index("c")
    chip = 2 * ax + ay
    dev = 2 * chip + ac
    c_arr = jnp.reshape(ac, (1,)).astype(jnp.int32)

    nc = d // 128
    pad8 = lambda a: jnp.pad(a, ((0, -a.shape[0] % 8), (0, 0)))
    small1 = jnp.concatenate([c.reshape(nc, 128), pad8(hg_lb.reshape(4, 128)), gla_w_gk.reshape(2 * RANK, 128),
                              pad8(gla_b_gk.reshape(2, 128))], axis=0)
    got1 = _allgather8(small1, "gather_small_params")
    c_all = got1[:, :nc, :].reshape(N_DEV, d)
    per_chip = got1[0::2]
    lb_full = per_chip[:, nc:nc + 4, :].transpose(1, 0, 2).reshape(4, HW)
    wgk_full = per_chip[:, nc + 8:nc + 8 + 2 * RANK, :].transpose(1, 0, 2).reshape(2, RANK, HW)
    bgk_full = per_chip[:, nc + 8 + 2 * RANK:nc + 10 + 2 * RANK, :].transpose(1, 0, 2).reshape(2, HW)
    wgk_pad = [jnp.zeros((128, HW), F32).at[dd * RANK:(dd + 1) * RANK].set(wgk_full[dd]) for dd in range(2)]
    bgk = [bgk_full[dd:dd + 1] for dd in range(2)]

    n_mod_cols = w_mod.shape[2]
    cond = jnp.concatenate([c_all, pad8(c_ctx.reshape(1, d))], axis=0)
    b_cols = lax.dynamic_slice(b_mod, (0, chip * n_mod_cols), (1, n_mod_cols))
    mod_part = _mod_forward(cond, w_mod[0], b_cols, "mod_forward")
    mod_got = _allgather8(mod_part, "gather_mod")
    mod_all = mod_got[0::2].transpose(1, 0, 2).reshape(16, N_CHIP * n_mod_cols)
    modx = pad8(lax.dynamic_slice(mod_all, (dev, 0), (1, N_MOD * d)).reshape(N_MOD, d))
    modc = pad8(mod_all[8].reshape(N_MOD, d))

    chip_arr = jnp.reshape(chip, (1,)).astype(jnp.int32)
    blocks = [_cast_into_blocks(chip_arr, w_[0], "cast_" + nm) for w_, nm in (
        (w_in, "w_in"), (w_br_hg, "w_br_hg"), (w_br_gla, "w_br_gla"), (w_out, "w_out"), (w_ff_gate, "w_ff_gate"),
        (w_ff_up, "w_ff_up"), (w_ff_down, "w_ff_down"))]
    gathered_in = _gather_blocks(blocks[:1], "gather_w_in")
    sems, lands, token = _blocks_start(blocks[1:], "gather_rest_start")
    w_in_r = _relayout_w_in(_unblocked(gathered_in[0]))

    norms = jnp.concatenate([norm_pre1, norm_post1, norm_pre2, norm_post2, jnp.zeros((4, d), F32)], axis=0)
    onorms = jnp.zeros((8, d), F32).at[0, :HD].set(hg_onorm[0]).at[1, :HD].set(gla_onorm[0])
    gla_side = [(wgk_pad[dd], bgk[dd]) for dd in range(2)]
    modx = modx + token[0, 0]
    front = _sample_front(x[0], ctx[0], modc, modx, norm_pre1, lb_full, gla_side, w_in_r)
    lands = _blocks_wait(sems, lands, front["o_list"][3], "gather_rest_wait")
    gathered = _blocks_finish(lands, "gather_rest_finish")
    wbh, wbg = _unblocked(gathered[0]), _unblocked(gathered[1])
    wout = gathered[2].reshape(d, d)
    wg, wu = _unblocked(gathered[3]), _unblocked(gathered[4])
    wd = gathered[5].reshape(wg.shape[1], d)
    dff = wg.shape[1]
    groups = {"ffn": ["w_ff_gate", "w_ff_up", "w_ff_down"], "mix": ["w_br_hg", "w_br_gla", "w_out"], "in": ["w_in"]}
    row_sharded = {"w_out": d // N_CHIP, "w_ff_down": dff // N_CHIP}
    in_flight = {}

    def reduce(group, grads):
        nms = groups[group]
        full = [g.reshape(N_CHIP, row_sharded[nm], d) if nm in row_sharded else _blocked(g, N_CHIP)
                for g, nm in zip(grads, nms)]
        from_sibling = _send_other_half(full, "grads_to_sibling_" + group)
        pairs = [_pair_sum(c_arr, f, r_, "pair_sum_" + nm) for f, r_, nm in zip(full, from_sibling, nms)]
        sems_, pairs, lands_, token_ = _scatter_start(pairs, "grads_to_owner_start_" + group)
        in_flight[group] = (sems_, pairs, lands_, token_)
        return token_[0, 0]

    r = _sample_back(reduce, front, x[0], ctx[0], loss_target[0], modc, modx, norm_pre1, norms, onorms, lb_full,
                     gla_side, w_in_r, wbh, wbg, wout, wg, wu, wd)
    loss_part, grad_x, stat_in, stat_mix, stat_ffn = (r[k] for k in ("loss_part", "grad_x", "stat_in", "stat_mix",
                                                                     "stat_ffn"))
    (dlb_f, dlb_b), (dwgk_f, dwgk_b), (dbgk_f, dbgk_b) = r["dlb"], r["dwgk"], r["dbgk"]

    weights = dict(w_in=(w_in, m_w_in, v_w_in), w_br_hg=(w_br_hg, m_w_br_hg, v_w_br_hg),
                   w_br_gla=(w_br_gla, m_w_br_gla, v_w_br_gla), w_out=(w_out, m_w_out, v_w_out),
                   w_ff_gate=(w_ff_gate, m_w_ff_gate, v_w_ff_gate), w_ff_up=(w_ff_up, m_w_ff_up, v_w_ff_up),
                   w_ff_down=(w_ff_down, m_w_ff_down, v_w_ff_down))
    names = ["w_in", "w_br_hg", "w_br_gla", "w_out", "w_ff_gate", "w_ff_up", "w_ff_down"]
    big = {}

    def finish(group, after):
        sems_, pairs, lands_, _ = in_flight[group]
        pairs, lands_ = _scatter_wait(sems_, pairs, lands_, after, "grads_to_owner_wait_" + group)
        own_half = [_sum_owner(chip_arr, pr, g, "chip_sum_" + nm) for pr, g, nm in zip(pairs, lands_, groups[group])]
        other_half = _swap_with_sibling(own_half, "halves_to_sibling_" + group)
        for nm, own, oth in zip(groups[group], own_half, other_half):
            w_, m_, v_ = weights[nm]
            res = _adamw_halves(c_arr, own, oth, w_[0], m_[0], v_[0], "adamw_" + nm)
            big[nm] = [r_[None] for r_ in res]
        return big[groups[group][-1]][1]

    done = finish("ffn", in_flight["in"][3])
    done = finish("mix", done)

    small2 = jnp.concatenate([
        stat_in, stat_mix, stat_ffn, jnp.concatenate([dlb_f, dlb_b], axis=1), jnp.concatenate([dbgk_f, dbgk_b], axis=1),
        jnp.concatenate([dwgk_f[0:RANK], dwgk_b[RANK:2 * RANK]], axis=1)], axis=0)
    assert small2.shape[0] == SMALL_ROWS
    got2 = _allgather8(small2, "gather_small_grads")
    total, dmod_all, g_b_mod, g_lb_full = _reduce_small(got2, lb_full, "reduce_small")
    dmod_cols = lax.dynamic_slice(dmod_all, (0, chip * n_mod_cols), (16, n_mod_cols))
    g_w_mod, cctx_part = _mod_backward(cond, w_mod[0], dmod_cols, "mod_backward")
    got3 = _allgather8(cctx_part, "gather_c_ctx_grad")
    g_c_ctx = _c_ctx_grad(got3, c_ctx.reshape(1, d), "c_ctx_grad")

    g_pre1, g_post1, g_pre2, g_post2 = (total[r_:r_ + 1] for r_ in (ROW_PRE1, ROW_POST1, ROW_PRE2, ROW_POST2))
    g_hg_on, g_gla_on = total[ROW_ONORM:ROW_ONORM + 1, 0:HD], total[ROW_ONORM:ROW_ONORM + 1, HD:2 * HD]
    n_lb = hg_lb.shape[2]
    g_hg_lb = lax.dynamic_slice(g_lb_full, (0, chip * n_lb), (4, n_lb))
    g_bgk = lax.dynamic_slice(total[ROW_BGK:ROW_BGK + 1].reshape(2, HW), (0, chip * n_lb), (2, n_lb))
    g_wgk_full = total[ROW_WGK:ROW_WGK + RANK].reshape(RANK, 2, HW).transpose(1, 0, 2).reshape(2 * RANK, HW)
    g_wgk = lax.dynamic_slice(g_wgk_full, (0, chip * n_lb), (2 * RANK, n_lb))

    small_items = [
        (g_c_ctx, c_ctx.reshape(1, d), m_c_ctx.reshape(1, d), v_c_ctx.reshape(1, d)),
        (g_b_mod, b_mod, m_b_mod, v_b_mod),
        (g_pre1, norm_pre1, m_norm_pre1, v_norm_pre1),
        (g_post1, norm_post1, m_norm_post1, v_norm_post1),
        (g_pre2, norm_pre2, m_norm_pre2, v_norm_pre2),
        (g_post2, norm_post2, m_norm_post2, v_norm_post2),
        (g_hg_lb, hg_lb.reshape(4, n_lb), m_hg_lb.reshape(4, n_lb), v_hg_lb.reshape(4, n_lb)),
        (g_hg_on, hg_onorm, m_hg_onorm, v_hg_onorm),
        (g_wgk, gla_w_gk.reshape(2 * RANK, n_lb), m_gla_w_gk.reshape(2 * RANK, n_lb), v_gla_w_gk.reshape(2 * RANK, n_lb)),
        (g_bgk, gla_b_gk.reshape(2, n_lb), m_gla_b_gk.reshape(2, n_lb), v_gla_b_gk.reshape(2, n_lb)),
        (g_gla_on, gla_onorm, m_gla_onorm, v_gla_onorm),
    ]
    small_res = _adamw_whole(small_items, "adamw_small")
    mod_res = _adamw_tiled(g_w_mod, w_mod[0], m_w_mod[0], v_w_mod[0], "adamw_w_mod")
    finish("in", mod_res[0])

    loss = lax.psum(loss_part[0, 0], ("x", "y", "c"))

    shapes = dict(c_ctx=c_ctx.shape, b_mod=b_mod.shape, norm_pre1=norm_pre1.shape, norm_post1=norm_post1.shape,
                  norm_pre2=norm_pre2.shape, norm_post2=norm_post2.shape, hg_lb=hg_lb.shape, hg_onorm=hg_onorm.shape,
                  gla_w_gk=gla_w_gk.shape, gla_b_gk=gla_b_gk.shape, gla_onorm=gla_onorm.shape)
    small_names = ["c_ctx", "b_mod", "norm_pre1", "norm_post1", "norm_pre2", "norm_post2", "hg_lb", "hg_onorm",
                   "gla_w_gk", "gla_b_gk", "gla_onorm"]
    grads, deltas, new_m, new_v = {}, {}, {}, {}
    for nm, item, res in zip(small_names, small_items, small_res):
        grads[nm] = item[0].reshape(shapes[nm])
        deltas[nm], new_m[nm], new_v[nm] = (r.reshape(shapes[nm]) for r in res)
    grads["w_mod"] = g_w_mod[None]
    deltas["w_mod"], new_m["w_mod"], new_v["w_mod"] = (r[None] for r in mod_res)
    for nm in names:
        grads[nm], deltas[nm], new_m[nm], new_v[nm] = big[nm]
    order = ["c_ctx", "w_mod", "b_mod", "norm_pre1", "norm_post1", "norm_pre2", "norm_post2", "w_in", "hg_lb",
             "hg_onorm", "gla_w_gk", "gla_b_gk", "gla_onorm", "w_br_hg", "w_br_gla", "w_out", "w_ff_gate", "w_ff_up",
             "w_ff_down"]
    return (loss, grad_x[None], *[grads[n] for n in order], *[deltas[n] for n in order],
            *[new_m[n] for n in order], *[new_v[n] for n in order])


def _weight_grad_cols(xs, dy, n_cols, name, tk=256, tn=512):
    rows = dy.shape[0]
    k = xs.shape[1]

    def body(x_ref, dy_ref, o_ref):
        o_ref[...] = _dot_tn(x_ref[...], dy_ref[...])

    return pl.pallas_call(
        body, name=name, grid=(k // tk, n_cols // tn),
        out_shape=jax.ShapeDtypeStruct((k, n_cols), F32),
        in_specs=[pl.BlockSpec((rows, tk), lambda i, j: (0, i)), pl.BlockSpec((rows, tn), lambda i, j: (0, j))],
        out_specs=pl.BlockSpec((tk, tn), lambda i, j: (i, j)),
        compiler_params=_cparams(dimension_semantics=("parallel", "parallel")),
    )(xs, dy)
```

```python
import functools

import jax
import jax.numpy as jnp
from jax import lax
from jax.experimental import pallas as pl
from jax.experimental.pallas import tpu as pltpu

F32 = jnp.float32
BF16 = jnp.bfloat16
HIGHEST = lax.Precision.HIGHEST
MESH = pl.DeviceIdType.MESH

EPS = 1e-6
CHUNK = 64
SUB = 16
NSUB = CHUNK // SUB
NH = 4
HD = 128
HW = NH * HD
RANK = 16
GATE_NORM = 16.0
N_MOD = 6
TM = 256
TM_FFN = 128
N_DEV = 8
N_CHIP = 4
VMEM_LIMIT = 56 * 1024 * 1024

ADAM_LR = 0.001
ADAM_B1 = 0.9
ADAM_B2 = 0.999
ADAM_EPS = 1e-08
ADAM_WD = 0.01
ADAM_STEP = 10

VMEM_SPEC = pl.BlockSpec(memory_space=pltpu.VMEM)
ANY_SPEC = pl.BlockSpec(memory_space=pl.ANY)
HBM_SPEC = pl.BlockSpec(memory_space=pltpu.HBM)
SEM_SPEC = pl.BlockSpec(memory_space=pltpu.SEMAPHORE)
EFFECT = pltpu.SideEffectType.DATAFLOW_SIDE_EFFECTING


def _cparams(**kw):
    return pltpu.CompilerParams(vmem_limit_bytes=VMEM_LIMIT, **kw)


def _dot(a, b):
    return jnp.dot(a.astype(BF16), b.astype(BF16), preferred_element_type=F32)


def _dot_nt(a, b):
    return lax.dot_general(a.astype(BF16), b.astype(BF16), (((1,), (1,)), ((), ())), preferred_element_type=F32)


def _dot_tn(a, b):
    return lax.dot_general(a.astype(BF16), b.astype(BF16), (((0,), (0,)), ((), ())), preferred_element_type=F32)


def _sigmoid(x):
    return 1.0 / (1.0 + jnp.exp(-x))


def _silu(x):
    return x * _sigmoid(x)


def _dsilu(x):
    s = _sigmoid(x)
    return s * (1.0 + x * (1.0 - s))


def _log_sigmoid(x):
    return jnp.minimum(x, 0.0) - jnp.log(1.0 + jnp.exp(-jnp.abs(x)))


def _colsum(a):
    return jnp.sum(a, axis=0, keepdims=True)


def _rms(a):
    r = lax.rsqrt(jnp.mean(a * a, axis=-1, keepdims=True) + EPS)
    return a * r, r


def _rms_bwd(dn, n, r):
    return r * (dn - n * jnp.mean(dn * n, axis=-1, keepdims=True))


def _place():
    x, y, c = lax.axis_index("x"), lax.axis_index("y"), lax.axis_index("c")
    chips = [(1 - x, y), (x, 1 - y), (1 - x, 1 - y)]
    return x, y, c, chips


def _allgather8(v, name):
    rows, cols = v.shape

    def body(x_ref, out_ref, send_sems, recv_sems, local_sem):
        x, y, c, chips = _place()
        me, sibling = (x, y, c), (x, y, 1 - c)

        def blk(px, py, pc):
            return out_ref.at[4 * px + 2 * py + pc]

        def copy(k, block, to, src=None):
            return pltpu.make_async_remote_copy(
                src_ref=blk(*block) if src is None else src, dst_ref=blk(*block),
                send_sem=send_sems.at[k], recv_sem=recv_sems.at[k], device_id=to, device_id_type=MESH)

        mine = pltpu.make_async_copy(x_ref, blk(*me), local_sem)
        mine.start()
        first = [copy(0, me, sibling, src=x_ref)]
        first += [copy(1 + j, me, (*chip, c), src=x_ref) for j, chip in enumerate(chips)]
        for cp in first:
            cp.start()
        passed = [copy(4 + j, (*chip, c), sibling) for j, chip in enumerate(chips)]
        for j, chip in enumerate(chips):
            copy(1 + j, (*chip, c), me).wait_recv()
            passed[j].start()
        copy(0, sibling, me).wait_recv()
        for j, chip in enumerate(chips):
            copy(4 + j, (*chip, 1 - c), me).wait_recv()
        for cp in first + passed:
            cp.wait_send()
        mine.wait()

    return pl.pallas_call(
        body, name=name,
        out_shape=jax.ShapeDtypeStruct((N_DEV, rows, cols), v.dtype),
        in_specs=[VMEM_SPEC], out_specs=VMEM_SPEC,
        scratch_shapes=[pltpu.SemaphoreType.DMA((7,)), pltpu.SemaphoreType.DMA((7,)), pltpu.SemaphoreType.DMA],
    )(v)


def _cast_into_blocks(chip_arr, w, name):
    rows, cols = w.shape
    tr = _row_tile(rows, 16, 256)

    def body(chip_ref, w_ref, o_ref):
        o_ref[0] = w_ref[...].astype(BF16)

    return pl.pallas_call(
        body, name=name,
        grid_spec=pltpu.PrefetchScalarGridSpec(
            num_scalar_prefetch=1, grid=(rows // tr,),
            in_specs=[pl.BlockSpec((tr, cols), lambda i, chip_ref: (i, 0))],
            out_specs=pl.BlockSpec((1, tr, cols), lambda i, chip_ref: (chip_ref[0], i, 0))),
        out_shape=jax.ShapeDtypeStruct((N_CHIP, rows, cols), BF16),
        compiler_params=_cparams(dimension_semantics=("parallel",)),
    )(chip_arr, w)


def _half_rows(ref, chip_id, pc):
    h = ref.shape[1] // 2
    return ref.at[chip_id, pl.ds(pl.multiple_of(pc * h, 8), h), :]


def _gather_blocks(lands, name):
    n = len(lands)

    def body(*refs):
        outs = refs[n:2 * n]
        send_sems, recv_sems = refs[2 * n:]
        x, y, c, chips = _place()
        me_chip = 2 * x + y
        sibling = (x, y, 1 - c)

        def copy(k, j, chip_id, pc, to):
            return pltpu.make_async_remote_copy(
                src_ref=_half_rows(outs[k], chip_id, pc), dst_ref=_half_rows(outs[k], chip_id, pc),
                send_sem=send_sems.at[k, j], recv_sem=recv_sems.at[k, j], device_id=to, device_id_type=MESH)

        started = []
        for k in range(n):
            for j, chip in enumerate(chips):
                cp = copy(k, j, me_chip, c, (*chip, c))
                cp.start()
                started.append(cp)
        for k in range(n):
            for j, (px, py) in enumerate(chips):
                copy(k, j, 2 * px + py, c, sibling).wait_recv()
                cp = copy(k, 3 + j, 2 * px + py, c, sibling)
                cp.start()
                started.append(cp)
        for k in range(n):
            for j, (px, py) in enumerate(chips):
                copy(k, 3 + j, 2 * px + py, 1 - c, sibling).wait_recv()
        for cp in started:
            cp.wait_send()

    return pl.pallas_call(
        body, name=name,
        out_shape=[jax.ShapeDtypeStruct(l.shape, l.dtype) for l in lands],
        in_specs=[ANY_SPEC] * n, out_specs=[ANY_SPEC] * n,
        input_output_aliases={i: i for i in range(n)},
        scratch_shapes=[pltpu.SemaphoreType.DMA((n, 6)), pltpu.SemaphoreType.DMA((n, 6))],
    )(*lands)


def _hbm(a):
    return pltpu.with_memory_space_constraint(a, pltpu.HBM)


def _blocks_start(lands, name):
    n = len(lands)
    n_sem = 3 * n

    def body(*refs):
        lnd = refs[:n]
        send_sems, recv_sems = refs[n:n + n_sem], refs[n + n_sem:n + 2 * n_sem]
        token = refs[-1]
        x, y, c, chips = _place()
        me_chip = 2 * x + y
        for k in range(n):
            for j, chip in enumerate(chips):
                pltpu.make_async_remote_copy(
                    src_ref=_half_rows(lnd[k], me_chip, c), dst_ref=_half_rows(lnd[k], me_chip, c),
                    send_sem=send_sems[3 * k + j], recv_sem=recv_sems[3 * k + j],
                    device_id=(*chip, c), device_id_type=MESH).start()
        token[...] = jnp.zeros_like(token)

    out = pl.pallas_call(
        body, name=name,
        out_shape=(*[pltpu.SemaphoreType.DMA(())] * (2 * n_sem),
                   *[pltpu.HBM(l.shape, l.dtype) for l in lands],
                   jax.ShapeDtypeStruct((8, 128), F32)),
        in_specs=[HBM_SPEC] * n,
        out_specs=(*[SEM_SPEC] * (2 * n_sem), *[HBM_SPEC] * n, VMEM_SPEC),
        input_output_aliases={i: 2 * n_sem + i for i in range(n)},
        compiler_params=pltpu.CompilerParams(has_side_effects=EFFECT),
    )(*[_hbm(l) for l in lands])
    return list(out[:2 * n_sem]), list(out[2 * n_sem:2 * n_sem + n]), out[-1]


def _blocks_wait(sems, lands, after, name):
    n = len(lands)
    n_sem = 3 * n

    def body(*refs):
        lnd = refs[:n]
        s_sems, r_sems = refs[n:n + n_sem], refs[n + n_sem:n + 2 * n_sem]
        x, y, c, chips = _place()
        me_chip = 2 * x + y
        for k in range(n):
            for j, (px, py) in enumerate(chips):
                cp = pltpu.make_async_remote_copy(
                    src_ref=_half_rows(lnd[k], me_chip, c), dst_ref=_half_rows(lnd[k], 2 * px + py, c),
                    send_sem=s_sems[3 * k + j], recv_sem=r_sems[3 * k + j],
                    device_id=(px, py, c), device_id_type=MESH)
                cp.wait_send()
                cp.wait_recv()

    out = pl.pallas_call(
        body, name=name,
        out_shape=tuple(pltpu.HBM(l.shape, l.dtype) for l in lands),
        in_specs=[HBM_SPEC] * n + [SEM_SPEC] * (2 * n_sem) + [ANY_SPEC],
        out_specs=[HBM_SPEC] * n,
        input_output_aliases={i: i for i in range(n)},
        compiler_params=pltpu.CompilerParams(has_side_effects=EFFECT),
    )(*lands, *sems, after)
    return list(out)


def _blocks_finish(lands, name):
    n = len(lands)

    def body(*refs):
        lnd = refs[n:2 * n]
        send_sems, recv_sems = refs[2 * n:]
        x, y, c, chips = _place()
        sibling = (x, y, 1 - c)

        def copy(k, j, chip_id, pc):
            return pltpu.make_async_remote_copy(
                src_ref=_half_rows(lnd[k], chip_id, pc), dst_ref=_half_rows(lnd[k], chip_id, pc),
                send_sem=send_sems.at[k, j], recv_sem=recv_sems.at[k, j], device_id=sibling, device_id_type=MESH)

        started = []
        for k in range(n):
            for j, (px, py) in enumerate(chips):
                cp = copy(k, j, 2 * px + py, c)
                cp.start()
                started.append(cp)
        for k in range(n):
            for j, (px, py) in enumerate(chips):
                copy(k, j, 2 * px + py, 1 - c).wait_recv()
        for cp in started:
            cp.wait_send()

    out = pl.pallas_call(
        body, name=name,
        out_shape=[jax.ShapeDtypeStruct(l.shape, l.dtype) for l in lands],
        in_specs=[ANY_SPEC] * n, out_specs=[ANY_SPEC] * n,
        input_output_aliases={i: i for i in range(n)},
        scratch_shapes=[pltpu.SemaphoreType.DMA((n, 3)), pltpu.SemaphoreType.DMA((n, 3))],
    )(*lands)
    return list(out)


def _gather_start(shards, name):
    n = len(shards)
    n_sem = 3 * n

    def body(*refs):
        ins, lands = refs[:n], refs[n:2 * n]
        send_sems, recv_sems = refs[2 * n:2 * n + n_sem], refs[2 * n + n_sem:2 * n + 2 * n_sem]
        token = refs[-1]
        x, y, c, chips = _place()
        me_chip = 2 * x + y
        for k in range(n):
            h = shards[k].shape[0] // 2
            rows = pl.ds(pl.multiple_of(c * h, 8), h)
            for j, chip in enumerate(chips):
                pltpu.make_async_remote_copy(
                    src_ref=ins[k].at[rows, :], dst_ref=lands[k].at[me_chip, rows, :],
                    send_sem=send_sems[3 * k + j], recv_sem=recv_sems[3 * k + j],
                    device_id=(*chip, c), device_id_type=MESH).start()
        token[...] = jnp.zeros_like(token)

    lands = [_hbm(lax.empty((N_CHIP,) + s.shape, s.dtype)) for s in shards]
    out = pl.pallas_call(
        body, name=name,
        out_shape=(*[pltpu.SemaphoreType.DMA(())] * (2 * n_sem),
                   *[pltpu.HBM(s.shape, s.dtype) for s in shards],
                   *[pltpu.HBM(l.shape, l.dtype) for l in lands],
                   jax.ShapeDtypeStruct((8, 128), F32)),
        in_specs=[HBM_SPEC] * (2 * n),
        out_specs=(*[SEM_SPEC] * (2 * n_sem), *[HBM_SPEC] * (2 * n), VMEM_SPEC),
        input_output_aliases={i: 2 * n_sem + i for i in range(2 * n)},
        compiler_params=pltpu.CompilerParams(has_side_effects=EFFECT),
    )(*[_hbm(s) for s in shards], *lands)
    sems = list(out[:2 * n_sem])
    return sems, list(out[2 * n_sem:2 * n_sem + n]), list(out[2 * n_sem + n:2 * n_sem + 2 * n]), out[-1]


def _gather_wait(sems, shards, lands, after, name):
    n = len(shards)
    n_sem = 3 * n

    def body(*refs):
        ins, lnd = refs[:n], refs[n:2 * n]
        s_sems, r_sems = refs[2 * n:2 * n + n_sem], refs[2 * n + n_sem:2 * n + 2 * n_sem]
        x, y, c, chips = _place()
        for k in range(n):
            h = shards[k].shape[0] // 2
            rows = pl.ds(pl.multiple_of(c * h, 8), h)
            for j, (px, py) in enumerate(chips):
                cp = pltpu.make_async_remote_copy(
                    src_ref=ins[k].at[rows, :], dst_ref=lnd[k].at[2 * px + py, rows, :],
                    send_sem=s_sems[3 * k + j], recv_sem=r_sems[3 * k + j],
                    device_id=(px, py, c), device_id_type=MESH)
                cp.wait_send()
                cp.wait_recv()

    out = pl.pallas_call(
        body, name=name,
        out_shape=(*[pltpu.HBM(s.shape, s.dtype) for s in shards], *[pltpu.HBM(l.shape, l.dtype) for l in lands]),
        in_specs=[HBM_SPEC] * (2 * n) + [SEM_SPEC] * (2 * n_sem) + [ANY_SPEC],
        out_specs=[HBM_SPEC] * (2 * n),
        input_output_aliases={i: i for i in range(2 * n)},
        compiler_params=pltpu.CompilerParams(has_side_effects=EFFECT),
    )(*shards, *lands, *sems, after)
    return list(out[:n]), list(out[n:])


def _gather_finish(shards, lands, name):
    n = len(shards)

    def body(*refs):
        ins, lnd = refs[:n], refs[2 * n:3 * n]
        send_sems, recv_sems, local_sems = refs[3 * n:]
        x, y, c, chips = _place()
        me_chip = 2 * x + y
        sibling = (x, y, 1 - c)

        def half(k, chip_id, pc):
            h = shards[k].shape[0] // 2
            return lnd[k].at[chip_id, pl.ds(pl.multiple_of(pc * h, 8), h), :]

        def copy(k, j, chip_id, pc):
            return pltpu.make_async_remote_copy(
                src_ref=half(k, chip_id, pc), dst_ref=half(k, chip_id, pc),
                send_sem=send_sems.at[k, j], recv_sem=recv_sems.at[k, j], device_id=sibling, device_id_type=MESH)

        locals_, started = [], []
        for k in range(n):
            cp = pltpu.make_async_copy(ins[k], lnd[k].at[me_chip], local_sems.at[k])
            cp.start()
            locals_.append(cp)
            for j, (px, py) in enumerate(chips):
                cp = copy(k, j, 2 * px + py, c)
                cp.start()
                started.append(cp)
        for k in range(n):
            for j, (px, py) in enumerate(chips):
                copy(k, j, 2 * px + py, 1 - c).wait_recv()
        for cp in started:
            cp.wait_send()
        for cp in locals_:
            cp.wait()

    out = pl.pallas_call(
        body, name=name,
        out_shape=[jax.ShapeDtypeStruct(l.shape, l.dtype) for l in lands],
        in_specs=[ANY_SPEC] * (2 * n), out_specs=[ANY_SPEC] * n,
        input_output_aliases={n + i: i for i in range(n)},
        scratch_shapes=[pltpu.SemaphoreType.DMA((n, 3)), pltpu.SemaphoreType.DMA((n, 3)),
                        pltpu.SemaphoreType.DMA((n,))],
    )(*shards, *lands)
    return list(out)


def _send_other_half(arrs, name):
    n = len(arrs)

    def body(*refs):
        ins, outs = refs[:n], refs[n:2 * n]
        send_sems, recv_sems = refs[2 * n:]
        x, y, c, _ = _place()
        cps = []
        for k in range(n):
            h = arrs[k].shape[1] // 2
            cp = pltpu.make_async_remote_copy(
                src_ref=ins[k].at[:, pl.ds(pl.multiple_of((1 - c) * h, 8), h), :], dst_ref=outs[k],
                send_sem=send_sems.at[k], recv_sem=recv_sems.at[k], device_id=(x, y, 1 - c), device_id_type=MESH)
            cp.start()
            cps.append(cp)
        for cp in cps:
            cp.wait()

    return pl.pallas_call(
        body, name=name,
        out_shape=[jax.ShapeDtypeStruct((a.shape[0], a.shape[1] // 2, a.shape[2]), a.dtype) for a in arrs],
        in_specs=[ANY_SPEC] * n, out_specs=[ANY_SPEC] * n,
        scratch_shapes=[pltpu.SemaphoreType.DMA((n,)), pltpu.SemaphoreType.DMA((n,))],
    )(*arrs)


def _blocks_to_owner(arrs, name):
    n = len(arrs)

    def body(*refs):
        ins, outs = refs[:n], refs[n:2 * n]
        send_sems, recv_sems, local_sems = refs[2 * n:]
        x, y, c, chips = _place()
        me_chip = 2 * x + y
        locals_, started = [], []
        for k in range(n):
            cp = pltpu.make_async_copy(ins[k].at[me_chip], outs[k].at[me_chip], local_sems.at[k])
            cp.start()
            locals_.append(cp)

        def copy(k, j, src_block, dst_slot, to):
            return pltpu.make_async_remote_copy(
                src_ref=ins[k].at[src_block], dst_ref=outs[k].at[dst_slot],
                send_sem=send_sems.at[k, j], recv_sem=recv_sems.at[k, j], device_id=to, device_id_type=MESH)

        for k in range(n):
            for j, (px, py) in enumerate(chips):
                cp = copy(k, j, 2 * px + py, me_chip, (px, py, c))
                cp.start()
                started.append(cp)
        for k in range(n):
            for j, (px, py) in enumerate(chips):
                copy(k, j, me_chip, 2 * px + py, (px, py, c)).wait_recv()
        for cp in started:
            cp.wait_send()
        for cp in locals_:
            cp.wait()

    return pl.pallas_call(
        body, name=name,
        out_shape=[jax.ShapeDtypeStruct(a.shape, a.dtype) for a in arrs],
        in_specs=[ANY_SPEC] * n, out_specs=[ANY_SPEC] * n,
        scratch_shapes=[pltpu.SemaphoreType.DMA((n, 3)), pltpu.SemaphoreType.DMA((n, 3)),
                        pltpu.SemaphoreType.DMA((n,))],
    )(*arrs)


def _scatter_blocks(arrs, name):
    n = len(arrs)

    def body(*refs):
        ins, outs = refs[:n], refs[n:2 * n]
        send_sems, recv_sems = refs[2 * n:]
        x, y, c, chips = _place()
        me_chip = 2 * x + y

        def copy(k, j, src_block, dst_slot, to):
            return pltpu.make_async_remote_copy(
                src_ref=ins[k].at[src_block], dst_ref=outs[k].at[dst_slot],
                send_sem=send_sems.at[k, j], recv_sem=recv_sems.at[k, j], device_id=to, device_id_type=MESH)

        started = []
        for k in range(n):
            for j, (px, py) in enumerate(chips):
                cp = copy(k, j, 2 * px + py, me_chip, (px, py, c))
                cp.start()
                started.append(cp)
        for k in range(n):
            for j, (px, py) in enumerate(chips):
                copy(k, j, me_chip, 2 * px + py, (px, py, c)).wait_recv()
        for cp in started:
            cp.wait_send()

    return pl.pallas_call(
        body, name=name,
        out_shape=[jax.ShapeDtypeStruct(a.shape, a.dtype) for a in arrs],
        in_specs=[ANY_SPEC] * n, out_specs=[ANY_SPEC] * n,
        scratch_shapes=[pltpu.SemaphoreType.DMA((n, 3)), pltpu.SemaphoreType.DMA((n, 3))],
    )(*arrs)


def _scatter_start(arrs, name):
    n = len(arrs)
    n_sem = 3 * n

    def body(*refs):
        ins, lnd = refs[:n], refs[n:2 * n]
        send_sems, recv_sems = refs[2 * n:2 * n + n_sem], refs[2 * n + n_sem:2 * n + 2 * n_sem]
        token = refs[-1]
        x, y, c, chips = _place()
        me_chip = 2 * x + y
        for k in range(n):
            for j, (px, py) in enumerate(chips):
                pltpu.make_async_remote_copy(
                    src_ref=ins[k].at[2 * px + py], dst_ref=lnd[k].at[me_chip],
                    send_sem=send_sems[3 * k + j], recv_sem=recv_sems[3 * k + j],
                    device_id=(px, py, c), device_id_type=MESH).start()
        token[...] = jnp.zeros_like(token)

    lands = [_hbm(lax.empty(a.shape, a.dtype)) for a in arrs]
    out = pl.pallas_call(
        body, name=name,
        out_shape=(*[pltpu.SemaphoreType.DMA(())] * (2 * n_sem),
                   *[pltpu.HBM(a.shape, a.dtype) for a in arrs], *[pltpu.HBM(a.shape, a.dtype) for a in arrs],
                   jax.ShapeDtypeStruct((8, 128), F32)),
        in_specs=[HBM_SPEC] * (2 * n),
        out_specs=(*[SEM_SPEC] * (2 * n_sem), *[HBM_SPEC] * (2 * n), VMEM_SPEC),
        input_output_aliases={i: 2 * n_sem + i for i in range(2 * n)},
        compiler_params=pltpu.CompilerParams(has_side_effects=EFFECT),
    )(*[_hbm(a) for a in arrs], *lands)
    base = 2 * n_sem
    return list(out[:base]), list(out[base:base + n]), list(out[base + n:base + 2 * n]), out[-1]


def _scatter_wait(sems, arrs, lands, after, name):
    n = len(arrs)
    n_sem = 3 * n

    def body(*refs):
        ins, lnd = refs[:n], refs[n:2 * n]
        s_sems, r_sems = refs[2 * n:2 * n + n_sem], refs[2 * n + n_sem:2 * n + 2 * n_sem]
        x, y, c, chips = _place()
        for k in range(n):
            for j, (px, py) in enumerate(chips):
                cp = pltpu.make_async_remote_copy(
                    src_ref=ins[k].at[2 * px + py], dst_ref=lnd[k].at[2 * px + py],
                    send_sem=s_sems[3 * k + j], recv_sem=r_sems[3 * k + j],
                    device_id=(px, py, c), device_id_type=MESH)
                cp.wait_send()
                cp.wait_recv()

    out = pl.pallas_call(
        body, name=name,
        out_shape=tuple(pltpu.HBM(a.shape, a.dtype) for a in list(arrs) + list(lands)),
        in_specs=[HBM_SPEC] * (2 * n) + [SEM_SPEC] * (2 * n_sem) + [ANY_SPEC] * len(after),
        out_specs=[HBM_SPEC] * (2 * n),
        input_output_aliases={i: i for i in range(2 * n)},
        compiler_params=pltpu.CompilerParams(has_side_effects=EFFECT),
    )(*arrs, *lands, *sems, *after)
    return list(out[:n]), list(out[n:])


def _sum_owner(chip_arr, pairs, got, name):
    nb, h, cols = got.shape
    tr = _row_tile(h, 16, 256)

    def body(chip_ref, own_ref, a_ref, b_ref, c_ref, o_ref):
        o_ref[...] = ((own_ref[0].astype(F32) + a_ref[0].astype(F32)) + b_ref[0].astype(F32)) + c_ref[0].astype(F32)

    def slot(off):
        return pl.BlockSpec((1, tr, cols), lambda i, chip_ref: ((chip_ref[0] + off) % N_CHIP, i, 0))

    return pl.pallas_call(
        body, name=name,
        grid_spec=pltpu.PrefetchScalarGridSpec(
            num_scalar_prefetch=1, grid=(h // tr,),
            in_specs=[slot(0), slot(1), slot(2), slot(3)],
            out_specs=pl.BlockSpec((tr, cols), lambda i, chip_ref: (i, 0))),
        out_shape=jax.ShapeDtypeStruct((h, cols), F32),
        compiler_params=_cparams(dimension_semantics=("parallel",)),
    )(chip_arr, pairs, got, got, got)


def _swap_with_sibling(arrs, name):
    n = len(arrs)

    def body(*refs):
        ins, outs = refs[:n], refs[n:2 * n]
        send_sems, recv_sems = refs[2 * n:]
        x, y, c, _ = _place()
        cps = []
        for k in range(n):
            cp = pltpu.make_async_remote_copy(
                src_ref=ins[k], dst_ref=outs[k], send_sem=send_sems.at[k], recv_sem=recv_sems.at[k],
                device_id=(x, y, 1 - c), device_id_type=MESH)
            cp.start()
            cps.append(cp)
        for cp in cps:
            cp.wait()

    return pl.pallas_call(
        body, name=name,
        out_shape=[jax.ShapeDtypeStruct(a.shape, a.dtype) for a in arrs],
        in_specs=[ANY_SPEC] * n, out_specs=[ANY_SPEC] * n,
        scratch_shapes=[pltpu.SemaphoreType.DMA((n,)), pltpu.SemaphoreType.DMA((n,))],
    )(*arrs)


def _row_tile(h, mult=8, cap=128):
    for t in range(cap - cap % mult, mult - 1, -mult):
        if h % t == 0:
            return t
    raise ValueError(h)


def _cast_bf16(a, name):
    rows, cols = a.shape
    tr = _row_tile(rows, 16, 256)

    def body(a_ref, o_ref):
        o_ref[...] = a_ref[...].astype(BF16)

    return pl.pallas_call(
        body, name=name, grid=(rows // tr,),
        out_shape=jax.ShapeDtypeStruct(a.shape, BF16),
        in_specs=[pl.BlockSpec((tr, cols), lambda i: (i, 0))],
        out_specs=pl.BlockSpec((tr, cols), lambda i: (i, 0)),
        compiler_params=_cparams(dimension_semantics=("parallel",)),
    )(a)


def _pair_sum(c_arr, full, recv, name):
    nb, rows, cols = full.shape
    h = rows // 2
    tr = _row_tile(h, 16, 256)
    steps = h // tr

    def body(c_ref, f_ref, r_ref, o_ref):
        o_ref[...] = (f_ref[...] + r_ref[...]).astype(BF16)

    return pl.pallas_call(
        body, name=name,
        grid_spec=pltpu.PrefetchScalarGridSpec(
            num_scalar_prefetch=1, grid=(nb, steps),
            in_specs=[pl.BlockSpec((1, tr, cols), lambda b, i, c_ref: (b, c_ref[0] * steps + i, 0)),
                      pl.BlockSpec((1, tr, cols), lambda b, i, c_ref: (b, i, 0))],
            out_specs=pl.BlockSpec((1, tr, cols), lambda b, i, c_ref: (b, i, 0))),
        out_shape=jax.ShapeDtypeStruct((nb, h, cols), BF16),
        compiler_params=_cparams(dimension_semantics=("parallel", "parallel")),
    )(c_arr, full, recv)


def _sum_chips(got, name):
    nb, h, cols = got.shape
    tr = _row_tile(h, 16, 256)

    def body(g_ref, o_ref):
        g = g_ref[...].astype(F32)
        o_ref[...] = ((g[0] + g[1]) + g[2]) + g[3]

    return pl.pallas_call(
        body, name=name, grid=(h // tr,),
        out_shape=jax.ShapeDtypeStruct((h, cols), F32),
        in_specs=[pl.BlockSpec((nb, tr, cols), lambda i: (0, i, 0))],
        out_specs=pl.BlockSpec((tr, cols), lambda i: (i, 0)),
        compiler_params=_cparams(dimension_semantics=("parallel",)),
    )(got)


def _adam_math(g, w, m, v):
    m1 = ADAM_B1 * m + (1.0 - ADAM_B1) * g
    v1 = ADAM_B2 * v + (1.0 - ADAM_B2) * (g * g)
    m_hat = m1 / (1.0 - ADAM_B1 ** ADAM_STEP)
    v_hat = v1 / (1.0 - ADAM_B2 ** ADAM_STEP)
    delta = -ADAM_LR * (m_hat / (jnp.sqrt(v_hat) + ADAM_EPS) + ADAM_WD * w)
    return delta, m1, v1


def _adamw_halves(c_arr, own, other, w, m, v, name):
    rows, cols = w.shape
    h = rows // 2
    tr = _row_tile(h)
    steps = h // tr

    def body(c_ref, own_ref, oth_ref, w_ref, m_ref, v_ref, g_out, d_out, m_out, v_out):
        g = jnp.where(pl.program_id(0) == c_ref[0], own_ref[...], oth_ref[...])
        d, m1, v1 = _adam_math(g, w_ref[...], m_ref[...], v_ref[...])
        g_out[...] = g
        d_out[...] = d
        m_out[...] = m1
        v_out[...] = v1

    half_spec = pl.BlockSpec((tr, cols), lambda p, i, c_ref: (i, 0))
    full_spec = pl.BlockSpec((tr, cols), lambda p, i, c_ref: (p * steps + i, 0))
    return pl.pallas_call(
        body, name=name,
        grid_spec=pltpu.PrefetchScalarGridSpec(
            num_scalar_prefetch=1, grid=(2, steps),
            in_specs=[half_spec, half_spec, full_spec, full_spec, full_spec],
            out_specs=[full_spec] * 4),
        out_shape=[jax.ShapeDtypeStruct(w.shape, F32)] * 4,
        compiler_params=_cparams(dimension_semantics=("parallel", "parallel")),
    )(c_arr, own, other, w, m, v)


def _adamw_whole(items, name):
    n = len(items)

    def body(*refs):
        ins, outs = refs[:4 * n], refs[4 * n:]
        for k in range(n):
            g, w, m, v = (r[...] for r in ins[4 * k:4 * k + 4])
            d, m1, v1 = _adam_math(g, w, m, v)
            outs[3 * k][...] = d
            outs[3 * k + 1][...] = m1
            outs[3 * k + 2][...] = v1

    flat = [a for it in items for a in it]
    shapes = [jax.ShapeDtypeStruct(it[1].shape, F32) for it in items for _ in range(3)]
    out = pl.pallas_call(
        body, name=name, out_shape=shapes,
        in_specs=[VMEM_SPEC] * (4 * n), out_specs=[VMEM_SPEC] * (3 * n),
        compiler_params=_cparams(),
    )(*flat)
    return [tuple(out[3 * k:3 * k + 3]) for k in range(n)]


def _adamw_tiled(g, w, m, v, name):
    rows, cols = w.shape
    tr = _row_tile(rows)

    def body(g_ref, w_ref, m_ref, v_ref, d_out, m_out, v_out):
        d, m1, v1 = _adam_math(g_ref[...], w_ref[...], m_ref[...], v_ref[...])
        d_out[...] = d
        m_out[...] = m1
        v_out[...] = v1

    spec = pl.BlockSpec((tr, cols), lambda i: (i, 0))
    return pl.pallas_call(
        body, name=name, grid=(rows // tr,),
        out_shape=[jax.ShapeDtypeStruct(w.shape, F32)] * 3,
        in_specs=[spec] * 4, out_specs=[spec] * 3,
        compiler_params=_cparams(dimension_semantics=("parallel",)),
    )(g, w, m, v)


def _mod_forward(cond, w_mod, b_mod_cols, name):
    def body(c_ref, w_ref, b_ref, o_ref):
        o_ref[...] = _dot(_silu(c_ref[...]), w_ref[...]) + b_ref[...]

    return pl.pallas_call(
        body, name=name, out_shape=jax.ShapeDtypeStruct((cond.shape[0], w_mod.shape[1]), F32),
        in_specs=[VMEM_SPEC] * 3, out_specs=VMEM_SPEC, compiler_params=_cparams(),
    )(cond, w_mod, b_mod_cols)


def _mod_backward(cond, w_mod, dmod_cols, name):
    def body(c_ref, w_ref, d_ref, gw_ref, gc_ref):
        s = _silu(c_ref[...])
        d = d_ref[...]
        gw_ref[...] = _dot_tn(s, d)
        gc_ref[...] = _dot_nt(d[8:16, :], w_ref[...])

    return pl.pallas_call(
        body, name=name,
        out_shape=[jax.ShapeDtypeStruct(w_mod.shape, F32), jax.ShapeDtypeStruct((8, w_mod.shape[0]), F32)],
        in_specs=[VMEM_SPEC] * 3, out_specs=[VMEM_SPEC] * 2, compiler_params=_cparams(),
    )(cond, w_mod, dmod_cols)


def _col_chunks(width, step=512):
    return [(s, min(step, width - s)) for s in range(0, width, step)]


def _in_projection(z, modc, modx, pre1, w_r, n_ctx_tiles, name):
    rows, d = z.shape
    width = w_r.shape[1]

    def body(z_ref, modc_ref, modx_ref, pre_ref, w_ref, h_ref, p_ref):
        is_ctx = pl.program_id(0) < n_ctx_tiles
        n, _ = _rms(z_ref[...])
        shift = jnp.where(is_ctx, modc_ref[0:1, :], modx_ref[0:1, :])
        scale = jnp.where(is_ctx, modc_ref[1:2, :], modx_ref[1:2, :])
        h = (n * pre_ref[...] * (1.0 + scale) + shift).astype(BF16)
        h_ref[...] = h
        for s, w in _col_chunks(width):
            p_ref[:, s:s + w] = jnp.dot(h, w_ref[:, s:s + w], preferred_element_type=F32)

    row = lambda i: (i, 0)
    fixed = lambda i: (0, 0)
    return pl.pallas_call(
        body, name=name, grid=(rows // TM,),
        out_shape=[jax.ShapeDtypeStruct((rows, d), BF16), jax.ShapeDtypeStruct((rows, width), F32)],
        in_specs=[pl.BlockSpec((TM, d), row), pl.BlockSpec((8, d), fixed), pl.BlockSpec((8, d), fixed),
                  pl.BlockSpec((1, d), fixed), VMEM_SPEC],
        out_specs=[pl.BlockSpec((TM, d), row), pl.BlockSpec((TM, width), row)],
        compiler_params=_cparams(dimension_semantics=("parallel",)),
    )(z, modc, modx, pre1, w_r)


C_HQ, C_HI, C_HF_FW, C_HF_BW, C_HGATE, C_GQ, C_GK, C_GV, C_GGATE = range(9)
OFF_GATE_HG = 9 * HW
OFF_LR = 13 * HW
P_WIDTH = OFF_LR + 128


def _head_norm_fwd(o, w):
    outs, ns, rs = [], [], []
    for h in range(NH):
        n, r = _rms(o[:, h * HD:(h + 1) * HD])
        ns.append(n)
        rs.append(r)
        outs.append(n * w)
    return jnp.concatenate(outs, axis=1), ns, rs


def _mixer_tail(z, o_hg, o_gla, p_hgate, p_ggate, p_gate_hg, p_gate_gla, hg_on, gla_on, wbh, wbg, wout):
    on_hg, n_hg, r_hg = _head_norm_fwd(o_hg, hg_on)
    on_gla, n_gla, r_gla = _head_norm_fwd(o_gla, gla_on)
    og_hg = (on_hg * _silu(p_hgate)).astype(BF16)
    og_gla = (on_gla * _silu(p_ggate)).astype(BF16)
    b_hg = jnp.dot(og_hg, wbh, preferred_element_type=F32)
    b_gla = jnp.dot(og_gla, wbg, preferred_element_type=F32)
    s_hg = _sigmoid(p_gate_hg)
    s_gla = _sigmoid(p_gate_gla)
    merged = (s_hg * b_hg + s_gla * b_gla).astype(BF16)
    y1 = jnp.dot(merged, wout, preferred_element_type=F32)
    return dict(on_hg=on_hg, n_hg=n_hg, r_hg=r_hg, on_gla=on_gla, n_gla=n_gla, r_gla=r_gla, og_hg=og_hg,
                og_gla=og_gla, b_hg=b_hg, b_gla=b_gla, s_hg=s_hg, s_gla=s_gla, merged=merged, y1=y1)


def _mixer_ffn(x_lat, p, o_list, modx, norms, onorms, w_br_hg, w_br_gla, w_out, w_gate, w_up, w_down, target,
               n_ctx_tiles, name):
    rows, d = x_lat.shape
    dff = w_gate.shape[1]
    inv_d = 1.0 / d

    def body(x_ref, ofw_hg, obw_hg, ofw_gla, obw_gla, p_hgate, p_ggate, p_ghg_a, p_ghg_b, p_ggla_a, p_ggla_b,
             modx_ref, norm_ref, on_ref, wbh_ref, wbg_ref, wout_ref, wg_ref, wu_ref, wd_ref, t_ref,
             loss_ref, dz2_ref, y1_ref, mrg_ref, oghg_ref, oggla_ref, h2_ref, a_ref, du_ref, dv_ref, dy2_ref,
             stat_ref):
        i = pl.program_id(0)
        post1, pre2, post2 = norm_ref[1:2, :], norm_ref[2:3, :], norm_ref[3:4, :]
        gate1, shift2, scale2, gate2 = modx_ref[2:3, :], modx_ref[3:4, :], modx_ref[4:5, :], modx_ref[5:6, :]
        p_gate_hg = jnp.concatenate([p_ghg_a[...], p_ghg_b[...]], axis=1)
        p_gate_gla = jnp.concatenate([p_ggla_a[...], p_ggla_b[...]], axis=1)
        t = _mixer_tail(x_ref[...], ofw_hg[...] + obw_hg[...], ofw_gla[...] + obw_gla[...], p_hgate[...],
                        p_ggate[...], p_gate_hg, p_gate_gla, on_ref[0:1, 0:HD], on_ref[1:2, 0:HD],
                        wbh_ref[...], wbg_ref[...], wout_ref[...])
        y1_ref[...] = t["y1"]
        mrg_ref[...] = t["merged"]
        oghg_ref[...] = t["og_hg"]
        oggla_ref[...] = t["og_gla"]
        n1, _ = _rms(t["y1"])
        z2 = x_ref[...] + n1 * post1 * gate1
        n2, r2 = _rms(z2)
        nw2 = n2 * pre2
        h2 = (nw2 * (1.0 + scale2) + shift2).astype(BF16)
        h2_ref[...] = h2
        u = jnp.dot(h2, wg_ref[...], preferred_element_type=F32)
        v = jnp.dot(h2, wu_ref[...], preferred_element_type=F32)
        su = _silu(u)
        a = (su * v).astype(BF16)
        a_ref[...] = a
        y2 = jnp.dot(a, wd_ref[...], preferred_element_type=F32)
        n3, r3 = _rms(y2)
        z3 = z2 + n3 * post2 * gate2
        err = z3 - t_ref[...]
        part = 0.5 * inv_d * jnp.sum(err * err)
        dz3 = err * inv_d
        dgate2 = _colsum(dz3 * n3 * post2)
        tt = dz3 * gate2
        dpost2 = _colsum(tt * n3)
        dy2 = _rms_bwd(tt * post2, n3, r3).astype(BF16)
        dy2_ref[...] = dy2
        da = _dot_nt(dy2, wd_ref[...])
        du = (da * v * _dsilu(u)).astype(BF16)
        dv = (da * su).astype(BF16)
        du_ref[...] = du
        dv_ref[...] = dv
        dh2 = _dot_nt(du, wg_ref[...]) + _dot_nt(dv, wu_ref[...])
        dshift2 = _colsum(dh2)
        dscale2 = _colsum(dh2 * nw2)
        dnw2 = dh2 * (1.0 + scale2)
        dpre2 = _colsum(dnw2 * n2)
        dz2_ref[...] = dz3 + _rms_bwd(dnw2 * pre2, n2, r2)

        @pl.when(i == 0)
        def _():
            stat_ref[...] = jnp.zeros_like(stat_ref)
            loss_ref[...] = jnp.zeros_like(loss_ref)

        for r, val in enumerate((dshift2, dscale2, dgate2, dpre2, dpost2)):
            stat_ref[r:r + 1, :] += val
        loss_ref[...] += part

    tm = TM_FFN
    ctx_tiles = n_ctx_tiles * (TM // tm)
    lat = lambda i: (i, 0)
    full = lambda i: (i + ctx_tiles, 0)
    fixed = lambda i: (0, 0)

    def pcol(blk):
        return pl.BlockSpec((tm, HW), lambda i: (i + ctx_tiles, blk))

    in_specs = ([pl.BlockSpec((tm, d), lat)] + [pl.BlockSpec((tm, HW), full)] * 4
                + [pcol(C_HGATE), pcol(C_GGATE), pcol(9), pcol(10), pcol(11), pcol(12)]
                + [pl.BlockSpec((8, d), fixed), pl.BlockSpec((8, d), fixed), pl.BlockSpec((8, d), fixed)]
                + [VMEM_SPEC] * 6 + [pl.BlockSpec((tm, d), lat)])
    bf = lambda w: jax.ShapeDtypeStruct((rows, w), BF16)
    out_shape = [jax.ShapeDtypeStruct((8, 128), F32), jax.ShapeDtypeStruct((rows, d), F32),
                 jax.ShapeDtypeStruct((rows, d), F32), bf(d), bf(HW), bf(HW), bf(d), bf(dff), bf(dff), bf(dff), bf(d),
                 jax.ShapeDtypeStruct((8, d), F32)]
    out_specs = [pl.BlockSpec((8, 128), fixed), pl.BlockSpec((tm, d), lat), pl.BlockSpec((tm, d), lat),
                 pl.BlockSpec((tm, d), lat), pl.BlockSpec((tm, HW), lat), pl.BlockSpec((tm, HW), lat),
                 pl.BlockSpec((tm, d), lat), pl.BlockSpec((tm, dff), lat), pl.BlockSpec((tm, dff), lat),
                 pl.BlockSpec((tm, dff), lat), pl.BlockSpec((tm, d), lat), pl.BlockSpec((8, d), fixed)]
    return pl.pallas_call(
        body, name=name, grid=(rows // tm,), out_shape=out_shape, in_specs=in_specs, out_specs=out_specs,
        compiler_params=_cparams(dimension_semantics=("arbitrary",)),
    )(x_lat, *o_list, p, p, p, p, p, p, modx, norms, onorms, w_br_hg, w_br_gla, w_out, w_gate, w_up, w_down, target)


def _mixer_tail_bwd(x_lat, p, o_list, dz2, y1, modx, norms, onorms, w_br_hg, w_br_gla, w_out, n_ctx_tiles, n_tiles,
                    name):
    rows, d = x_lat.shape
    total = n_tiles * TM

    def body(x_ref, ofw_hg, obw_hg, ofw_gla, obw_gla, p_hgate, p_ggate, p_ghg_a, p_ghg_b, p_ggla_a, p_ggla_b,
             dz2_ref, y1_ref, modx_ref, norm_ref, on_ref, wbh_ref, wbg_ref, wout_ref,
             dohg_ref, dogla_ref, dhgate_ref, dggate_ref, dghg_ref, dggla_ref, dy1_ref, dbhg_ref, dbgla_ref,
             stat_ref):
        i = pl.program_id(0)

        @pl.when(i == 0)
        def _():
            stat_ref[...] = jnp.zeros_like(stat_ref)

        @pl.when(i < n_ctx_tiles)
        def _():
            for ref in (dohg_ref, dogla_ref, dhgate_ref, dggate_ref, dghg_ref, dggla_ref):
                ref[...] = jnp.zeros_like(ref)

        @pl.when(i >= n_ctx_tiles)
        def _():
            post1, gate1 = norm_ref[1:2, :], modx_ref[2:3, :]
            hg_on, gla_on = on_ref[0:1, 0:HD], on_ref[1:2, 0:HD]
            p_gate_hg = jnp.concatenate([p_ghg_a[...], p_ghg_b[...]], axis=1)
            p_gate_gla = jnp.concatenate([p_ggla_a[...], p_ggla_b[...]], axis=1)
            ph, pg = p_hgate[...], p_ggate[...]
            t = _mixer_tail(x_ref[...], ofw_hg[...] + obw_hg[...], ofw_gla[...] + obw_gla[...], ph, pg,
                            p_gate_hg, p_gate_gla, hg_on, gla_on, wbh_ref[...], wbg_ref[...], wout_ref[...])
            dz2 = dz2_ref[...]
            n1, r1 = _rms(y1_ref[...])
            dgate1 = _colsum(dz2 * n1 * post1)
            tt = dz2 * gate1
            dpost1 = _colsum(tt * n1)
            dy1 = _rms_bwd(tt * post1, n1, r1).astype(BF16)
            dy1_ref[...] = dy1
            dmerged = _dot_nt(dy1, wout_ref[...])
            dghg_ref[...] = dmerged * t["b_hg"] * t["s_hg"] * (1.0 - t["s_hg"])
            dggla_ref[...] = dmerged * t["b_gla"] * t["s_gla"] * (1.0 - t["s_gla"])
            db_hg = (dmerged * t["s_hg"]).astype(BF16)
            db_gla = (dmerged * t["s_gla"]).astype(BF16)
            dbhg_ref[...] = db_hg
            dbgla_ref[...] = db_gla
            don_acc = []
            for (db, wb, pgate, on, ns, rs, gain, gate_ref, do_ref) in (
                    (db_hg, wbh_ref, ph, t["on_hg"], t["n_hg"], t["r_hg"], hg_on, dhgate_ref, dohg_ref),
                    (db_gla, wbg_ref, pg, t["on_gla"], t["n_gla"], t["r_gla"], gla_on, dggate_ref, dogla_ref)):
                dog = _dot_nt(db, wb[...])
                gate_ref[...] = dog * on * _dsilu(pgate)
                don = dog * _silu(pgate)
                acc = jnp.zeros((1, HD), F32)
                for h in range(NH):
                    sl = slice(h * HD, (h + 1) * HD)
                    acc = acc + _colsum(don[:, sl] * ns[h])
                    do_ref[:, sl] = _rms_bwd(don[:, sl] * gain, ns[h], rs[h])
                don_acc.append(acc)
            stat_ref[0:1, :] += dgate1
            stat_ref[1:2, :] += dpost1
            stat_ref[2:3, 0:HD] += don_acc[0]
            stat_ref[2:3, HD:2 * HD] += don_acc[1]

    lat = lambda i: (jnp.maximum(i - n_ctx_tiles, 0), 0)
    full = lambda i: (i, 0)
    fixed = lambda i: (0, 0)

    def pcol(blk):
        return pl.BlockSpec((TM, HW), lambda i: (i, blk))

    in_specs = ([pl.BlockSpec((TM, d), lat)] + [pl.BlockSpec((TM, HW), full)] * 4
                + [pcol(C_HGATE), pcol(C_GGATE), pcol(9), pcol(10), pcol(11), pcol(12)]
                + [pl.BlockSpec((TM, d), lat), pl.BlockSpec((TM, d), lat)]
                + [pl.BlockSpec((8, d), fixed)] * 3 + [VMEM_SPEC] * 3)
    f = lambda w: jax.ShapeDtypeStruct((total, w), F32)
    out_shape = [f(HW), f(HW), f(HW), f(HW), f(d), f(d), jax.ShapeDtypeStruct((rows, d), BF16),
                 jax.ShapeDtypeStruct((rows, d), BF16), jax.ShapeDtypeStruct((rows, d), BF16),
                 jax.ShapeDtypeStruct((8, d), F32)]
    out_specs = ([pl.BlockSpec((TM, HW), full)] * 4 + [pl.BlockSpec((TM, d), full)] * 2
                 + [pl.BlockSpec((TM, d), lat)] * 3 + [pl.BlockSpec((8, d), fixed)])
    return pl.pallas_call(
        body, name=name, grid=(n_tiles,), out_shape=out_shape, in_specs=in_specs, out_specs=out_specs,
        compiler_params=_cparams(dimension_semantics=("arbitrary",)),
    )(x_lat, *o_list, p, p, p, p, p, p, dz2, y1, modx, norms, onorms, w_br_hg, w_br_gla, w_out)


def _in_projection_bwd(z, dz2, modc, modx, pre1, w_r, pieces, n_ctx_tiles, name):
    rows, d = z.shape
    lat_rows = dz2.shape[0]
    width = w_r.shape[1]
    n_pieces = len(pieces)

    def body(*refs):
        z_ref, dz2_ref, modc_ref, modx_ref, pre_ref, w_ref = refs[:6]
        (dhq_f, dhq_b, dhi_f, dhi_b, dhf_f, dhf_b, dhgate, dgq_f, dgq_b, dgk_f, dgk_b, dgv_f, dgv_b, dggate,
         dghg, dggla, dlr_f, dlr_b) = refs[6:6 + n_pieces]
        dp_ref, gx_ref, stat_ref = refs[6 + n_pieces:]
        i = pl.program_id(0)
        is_ctx = i < n_ctx_tiles
        sections = [
            (0, dhq_f[...] + dhq_b[...]), (HW, dhi_f[...] + dhi_b[...]), (2 * HW, dhf_f[...]), (3 * HW, dhf_b[...]),
            (4 * HW, dhgate[...]), (5 * HW, dgq_f[...] + dgq_b[...]), (6 * HW, dgk_f[...] + dgk_b[...]),
            (7 * HW, dgv_f[...] + dgv_b[...]), (8 * HW, dggate[...]),
            (9 * HW, dghg[:, 0:HW]), (10 * HW, dghg[:, HW:2 * HW]),
            (11 * HW, dggla[:, 0:HW]), (12 * HW, dggla[:, HW:2 * HW]), (OFF_LR, dlr_f[...] + dlr_b[...])]
        dh = jnp.zeros((TM, d), F32)
        for off, val in sections:
            w = val.shape[1]
            vb = val.astype(BF16)
            dp_ref[:, off:off + w] = vb
            dh = dh + _dot_nt(vb, w_ref[:, off:off + w])
        n, r = _rms(z_ref[...])
        pre = pre_ref[...]
        scale = jnp.where(is_ctx, modc_ref[1:2, :], modx_ref[1:2, :])
        nw = n * pre
        dshift = _colsum(dh)
        dscale = _colsum(dh * nw)
        dnw = dh * (1.0 + scale)
        dpre = _colsum(dnw * n)
        gx_ref[...] = dz2_ref[...] + _rms_bwd(dnw * pre, n, r)
        zero = jnp.zeros((1, d), F32)

        @pl.when(i == 0)
        def _():
            stat_ref[...] = jnp.zeros_like(stat_ref)

        stat_ref[0:1, :] += jnp.where(is_ctx, zero, dshift)
        stat_ref[1:2, :] += jnp.where(is_ctx, zero, dscale)
        stat_ref[2:3, :] += jnp.where(is_ctx, dshift, zero)
        stat_ref[3:4, :] += jnp.where(is_ctx, dscale, zero)
        stat_ref[4:5, :] += dpre

    full = lambda i: (i, 0)
    lat = lambda i: (jnp.maximum(i - n_ctx_tiles, 0), 0)
    fixed = lambda i: (0, 0)
    piece_specs = [pl.BlockSpec((TM, a.shape[1]), full) for a in pieces]
    in_specs = [pl.BlockSpec((TM, d), full), pl.BlockSpec((TM, d), lat), pl.BlockSpec((8, d), fixed),
                pl.BlockSpec((8, d), fixed), pl.BlockSpec((1, d), fixed), VMEM_SPEC] + piece_specs
    return pl.pallas_call(
        body, name=name, grid=(rows // TM,),
        out_shape=[jax.ShapeDtypeStruct((rows, width), BF16), jax.ShapeDtypeStruct((lat_rows, d), F32),
                   jax.ShapeDtypeStruct((8, d), F32)],
        in_specs=in_specs,
        out_specs=[pl.BlockSpec((TM, width), full), pl.BlockSpec((TM, d), lat), pl.BlockSpec((8, d), fixed)],
        compiler_params=_cparams(dimension_semantics=("arbitrary",)),
    )(z, dz2, modc, modx, pre1, w_r, *pieces)


def _weight_grad(xs, dy, name, col_block=None, tk=256, tn=512):
    rows = dy.shape[0]
    k = xs.shape[1]
    if col_block is None:
        n, tn_, cb = dy.shape[1], tn, 0
    else:
        n, tn_, cb = col_block[0], col_block[0], col_block[1]
    tn_ = min(tn_, n)
    tk_ = min(tk, k)

    def body(x_ref, dy_ref, o_ref):
        o_ref[...] = _dot_tn(x_ref[...], dy_ref[...])

    return pl.pallas_call(
        body, name=name, grid=(k // tk_, n // tn_),
        out_shape=jax.ShapeDtypeStruct((k, n), F32),
        in_specs=[pl.BlockSpec((rows, tk_), lambda i, j: (0, i)),
                  pl.BlockSpec((rows, tn_), lambda i, j: (0, j + cb))],
        out_specs=pl.BlockSpec((tk_, tn_), lambda i, j: (i, j)),
        compiler_params=_cparams(dimension_semantics=("parallel", "parallel")),
    )(xs, dy)


def _chunk_terms(q, k, g, fw):
    c = CHUNK
    r = lax.broadcasted_iota(jnp.int32, (c, c), 0)
    s = lax.broadcasted_iota(jnp.int32, (c, c), 1)
    causal = (s <= r) if fw else (s >= r)
    causal_t = (s >= r) if fw else (s <= r)
    cum = jnp.dot(causal.astype(F32), g, precision=HIGHEST, preferred_element_type=F32)
    row = lax.broadcasted_iota(jnp.int32, (c, 1), 0)
    pos = row if fw else (c - 1 - row)
    starts = [None]
    for j in range(1, NSUB):
        rj = SUB * j - 1 if fw else c - SUB * j
        starts.append(cum[rj:rj + 1, :])
    in_blk = [(pos >= SUB * j) & (pos < SUB * (j + 1)) for j in range(NSUB)]
    e = [jnp.exp(cum)]
    for j in range(1, NSUB):
        e.append(jnp.exp(jnp.where(pos >= SUB * j, cum - starts[j], -1e30)))
    own = jnp.zeros_like(cum)
    for j in range(1, NSUB):
        own = own + jnp.where(in_blk[j], starts[j], 0.0)
    kscale = jnp.exp(own - cum)
    rend = c - 1 if fw else 0
    cend = cum[rend:rend + 1, :]
    tail = jnp.exp(cend - cum)
    qcat = jnp.concatenate([q * e[j] for j in range(NSUB)], axis=1).astype(BF16)
    kt = k * kscale
    km = jnp.concatenate([jnp.where(in_blk[j], kt, 0.0) for j in range(NSUB)], axis=1).astype(BF16)
    return dict(causal=causal, causal_t=causal_t, e=e, in_blk=in_blk, kscale=kscale, cend=cend, tail=tail,
                qcat=qcat, km=km)


def _chunk_fwd(q, k, v, g, st0, fw):
    t = _chunk_terms(q, k, g, fw)
    a = jnp.where(t["causal"], _dot_nt(t["qcat"], t["km"]), 0.0)
    o = _dot(a, v) + _dot_nt(t["qcat"][:, 0:HD], st0)
    st1 = st0 * jnp.exp(t["cend"]) + _dot_tn(v, k * t["tail"])
    return o, st1


def _chunk_bwd(q, k, v, g, st0, do, dst1, fw):
    t = _chunk_terms(q, k, g, fw)
    qcat, km, e = t["qcat"], t["km"], t["e"]
    a_t = jnp.where(t["causal_t"], _dot_nt(km, qcat), 0.0)
    ktail = k * t["tail"]
    dv = _dot(a_t, do) + _dot_nt(ktail, dst1)
    da = jnp.where(t["causal"], _dot_nt(do, v), 0.0)
    da_t = jnp.where(t["causal_t"], _dot_nt(v, do), 0.0)
    dqcat = _dot(da, km)
    dq_inter = e[0] * _dot(do, st0)
    dq = dq_inter
    for j in range(NSUB):
        dq = dq + e[j] * dqcat[:, j * HD:(j + 1) * HD]
    dkm = _dot(da_t, qcat)
    dkt = jnp.zeros_like(k)
    for j in range(NSUB):
        dkt = dkt + jnp.where(t["in_blk"][j], dkm[:, j * HD:(j + 1) * HD], 0.0)
    dk_inter = _dot(v, dst1) * t["tail"]
    dk = dkt * t["kscale"] + dk_inter
    dcum = q * dq_inter - k * dk_inter
    for j in range(NSUB):
        sl = slice(j * HD, (j + 1) * HD)
        dcum = dcum + qcat[:, sl].astype(F32) * dqcat[:, sl] - km[:, sl].astype(F32) * dkm[:, sl]
    ecend = jnp.exp(t["cend"])
    end = ecend * _colsum(st0 * dst1) + _colsum(k * dk_inter)
    dg = jnp.dot(t["causal_t"].astype(F32), dcum, precision=HIGHEST, preferred_element_type=F32) + end
    dst0 = dst1 * ecend + _dot_tn(do, q * e[0])
    return dq, dk, dv, dg, dst0


def _chunk_index(step, n_ctx_chunks, n_chunks, fw):
    if fw:
        return step
    return jnp.where(step < n_ctx_chunks, n_ctx_chunks - 1 - step, n_chunks - 1 + n_ctx_chunks - step)


def _hg_inputs(hq, hf, lbv, d_idx, sl):
    lb = _sigmoid(lbv[d_idx:d_idx + 1, sl] - lbv[2 + d_idx:3 + d_idx, sl])
    sg = _sigmoid(hf)
    f = lb + (1.0 - lb) * sg
    return _silu(hq), 1.0 - f, jnp.log(f), f, sg, lb


def _scan_fwd(p, side, n_ctx_chunks, fw, branch, name):
    rows = p.shape[0]
    n_chunks = rows // CHUNK
    d_idx = 0 if fw else 1
    hg = branch == "hg"
    cols = (C_HQ, C_HI, C_HF_FW + d_idx) if hg else (C_GQ, C_GK, C_GV)

    def body(*refs):
        if hg:
            a_ref, b_ref, c_ref, lb_ref, o_ref, st_ref, state = refs
        else:
            a_ref, b_ref, c_ref, lr_ref, wgk_ref, bgk_ref, o_ref, st_ref, state = refs
            logits = _dot(lr_ref[...], wgk_ref[...]) + bgk_ref[...]
            g_all = _log_sigmoid(logits) * (1.0 / GATE_NORM)

        @pl.when(pl.program_id(0) == 0)
        def _():
            state[...] = jnp.zeros_like(state)

        for h in range(NH):
            sl = slice(h * HD, (h + 1) * HD)
            if hg:
                q, k, g, _, _, _ = _hg_inputs(a_ref[:, sl], c_ref[:, sl], lb_ref[...], d_idx, sl)
                v = b_ref[:, sl]
            else:
                q, k, v, g = a_ref[:, sl] * (HD ** -0.5), b_ref[:, sl], c_ref[:, sl], g_all[:, sl]
            st0 = state[h]
            st_ref[0, h] = st0
            o, st1 = _chunk_fwd(q, k, v, g, st0, fw)
            o_ref[:, sl] = o
            state[h] = st1

    def cmap(blk):
        return pl.BlockSpec((CHUNK, HW), lambda j: (_chunk_index(j, n_ctx_chunks, n_chunks, fw), blk))

    fixed = lambda j: (0, 0)
    in_specs = [cmap(cols[0]), cmap(cols[1]), cmap(cols[2])]
    if hg:
        in_specs += [pl.BlockSpec((4, HW), fixed)]
        args = (p, p, p, side)
    else:
        in_specs += [pl.BlockSpec((CHUNK, 128), lambda j: (_chunk_index(j, n_ctx_chunks, n_chunks, fw), OFF_LR // 128)),
                     pl.BlockSpec((128, HW), fixed), pl.BlockSpec((1, HW), fixed)]
        args = (p, p, p, p, side[0], side[1])
    return pl.pallas_call(
        body, name=name, grid=(n_chunks,),
        out_shape=[jax.ShapeDtypeStruct((rows, HW), F32), jax.ShapeDtypeStruct((n_chunks, NH, HD, HD), F32)],
        in_specs=in_specs,
        out_specs=[pl.BlockSpec((CHUNK, HW), lambda j: (_chunk_index(j, n_ctx_chunks, n_chunks, fw), 0)),
                   pl.BlockSpec((1, NH, HD, HD), lambda j: (_chunk_index(j, n_ctx_chunks, n_chunks, fw), 0, 0, 0))],
        scratch_shapes=[pltpu.VMEM((NH, HD, HD), F32)],
        compiler_params=_cparams(dimension_semantics=("arbitrary",)),
    )(*args)


def _scan_bwd(p, side, states, d_o, n_ctx_chunks, fw, branch, name):
    rows = p.shape[0]
    n_chunks = rows // CHUNK
    d_idx = 0 if fw else 1
    hg = branch == "hg"
    cols = (C_HQ, C_HI, C_HF_FW + d_idx) if hg else (C_GQ, C_GK, C_GV)

    def body(*refs):
        if hg:
            a_ref, b_ref, c_ref, lb_ref, st_ref, do_ref, da_ref, db_ref, dc_ref, dlb_ref, dstate = refs
        else:
            (a_ref, b_ref, c_ref, lr_ref, wgk_ref, bgk_ref, st_ref, do_ref, da_ref, db_ref, dc_ref, dlr_ref,
             dwgk_ref, dbias_ref, dstate) = refs
            lr = lr_ref[...]
            logits = _dot(lr, wgk_ref[...]) + bgk_ref[...]
            g_all = _log_sigmoid(logits) * (1.0 / GATE_NORM)

        @pl.when(pl.program_id(0) == 0)
        def _():
            dstate[...] = jnp.zeros_like(dstate)
            if hg:
                dlb_ref[...] = jnp.zeros_like(dlb_ref)
            else:
                dwgk_ref[...] = jnp.zeros_like(dwgk_ref)
                dbias_ref[...] = jnp.zeros_like(dbias_ref)

        dg_parts = []
        for h in range(NH):
            sl = slice(h * HD, (h + 1) * HD)
            if hg:
                hq, hf = a_ref[:, sl], c_ref[:, sl]
                q, k, g, f, sg, lb = _hg_inputs(hq, hf, lb_ref[...], d_idx, sl)
                v = b_ref[:, sl]
            else:
                q, k, v, g = a_ref[:, sl] * (HD ** -0.5), b_ref[:, sl], c_ref[:, sl], g_all[:, sl]
            dq, dk, dv, dg, dst0 = _chunk_bwd(q, k, v, g, st_ref[0, h], do_ref[:, sl], dstate[h], fw)
            dstate[h] = dst0
            if hg:
                da_ref[:, sl] = dq * _dsilu(hq)
                db_ref[:, sl] = dv
                df = dg / f - dk
                dc_ref[:, sl] = df * (1.0 - lb) * sg * (1.0 - sg)
                dlb_ref[0:1, sl] += _colsum(df * (1.0 - sg))
            else:
                da_ref[:, sl] = dq * (HD ** -0.5)
                db_ref[:, sl] = dk
                dc_ref[:, sl] = dv
                dg_parts.append(dg)
        if not hg:
            dlogits = jnp.concatenate(dg_parts, axis=1) * (1.0 / GATE_NORM) * (1.0 - _sigmoid(logits))
            dlr_ref[...] = _dot_nt(dlogits, wgk_ref[...])
            dwgk_ref[...] += _dot_tn(lr, dlogits)
            dbias_ref[0:1, :] += _colsum(dlogits)

    def chunk_of(j):
        return _chunk_index(n_chunks - 1 - j, n_ctx_chunks, n_chunks, fw)

    def cmap(blk, width=HW):
        return pl.BlockSpec((CHUNK, width), lambda j: (chunk_of(j), blk))

    fixed = lambda j: (0, 0)
    st_spec = pl.BlockSpec((1, NH, HD, HD), lambda j: (chunk_of(j), 0, 0, 0))
    big = jax.ShapeDtypeStruct((rows, HW), F32)
    if hg:
        in_specs = [cmap(cols[0]), cmap(cols[1]), cmap(cols[2]), pl.BlockSpec((4, HW), fixed), st_spec, cmap(0)]
        args = (p, p, p, side, states, d_o)
        out_shape = [big, big, big, jax.ShapeDtypeStruct((8, HW), F32)]
        out_specs = [cmap(0), cmap(0), cmap(0), pl.BlockSpec((8, HW), fixed)]
    else:
        in_specs = [cmap(cols[0]), cmap(cols[1]), cmap(cols[2]), cmap(OFF_LR // 128, 128),
                    pl.BlockSpec((128, HW), fixed), pl.BlockSpec((1, HW), fixed), st_spec, cmap(0)]
        args = (p, p, p, p, side[0], side[1], states, d_o)
        out_shape = [big, big, big, jax.ShapeDtypeStruct((rows, 128), F32), jax.ShapeDtypeStruct((128, HW), F32),
                     jax.ShapeDtypeStruct((8, HW), F32)]
        out_specs = [cmap(0), cmap(0), cmap(0), cmap(0, 128), pl.BlockSpec((128, HW), fixed),
                     pl.BlockSpec((8, HW), fixed)]
    return pl.pallas_call(
        body, name=name, grid=(n_chunks,), out_shape=out_shape, in_specs=in_specs, out_specs=out_specs,
        scratch_shapes=[pltpu.VMEM((NH, HD, HD), F32)],
        compiler_params=_cparams(dimension_semantics=("arbitrary",)),
    )(*args)


SMALL_ROWS = 56
ROWS_MOD_X = (0, 1, 8, 16, 17, 18)
ROWS_MOD_C = (2, 3)
ROW_PRE1, ROW_POST1, ROW_ONORM, ROW_PRE2, ROW_POST2, ROW_LB, ROW_BGK, ROW_WGK = 4, 9, 10, 19, 20, 24, 32, 40


def _reduce_small(gathered, lb_full, name):
    _, _, d = gathered.shape

    def body(g_ref, lb_ref, sum_ref, dmod_ref, dbmod_ref, dlb_ref):
        total = g_ref[0]
        for b in range(1, N_DEV):
            total = total + g_ref[b]
        sum_ref[...] = total
        dmod_ref[...] = jnp.zeros_like(dmod_ref)
        for m in range(N_MOD):
            col = slice(m * d, (m + 1) * d)
            acc = jnp.zeros((1, d), F32)
            for b in range(N_DEV):
                row = g_ref[b, ROWS_MOD_X[m]:ROWS_MOD_X[m] + 1, :]
                dmod_ref[b:b + 1, col] = row
                acc = acc + row
            if m < 2:
                ctx_row = total[ROWS_MOD_C[m]:ROWS_MOD_C[m] + 1, :]
                dmod_ref[8:9, col] = ctx_row
                acc = acc + ctx_row
            dbmod_ref[:, col] = acc
        lbv = lb_ref[...]
        for dd in range(2):
            lb = _sigmoid(lbv[dd:dd + 1, :] - lbv[2 + dd:3 + dd, :])
            gl = total[ROW_LB:ROW_LB + 1, dd * HW:(dd + 1) * HW] * lb * (1.0 - lb)
            dlb_ref[dd:dd + 1, :] = gl
            dlb_ref[2 + dd:3 + dd, :] = -gl

    return pl.pallas_call(
        body, name=name,
        out_shape=[jax.ShapeDtypeStruct((SMALL_ROWS, d), F32), jax.ShapeDtypeStruct((16, N_MOD * d), F32),
                   jax.ShapeDtypeStruct((1, N_MOD * d), F32), jax.ShapeDtypeStruct((4, HW), F32)],
        in_specs=[VMEM_SPEC] * 2, out_specs=[VMEM_SPEC] * 4, compiler_params=_cparams(),
    )(gathered, lb_full)


def _c_ctx_grad(gathered, c_ctx_row, name):
    def body(g_ref, c_ref, o_ref):
        acc = g_ref[0, 0:1, :]
        for chip in range(1, N_CHIP):
            acc = acc + g_ref[2 * chip, 0:1, :]
        o_ref[...] = acc * _dsilu(c_ref[...])

    return pl.pallas_call(
        body, name=name, out_shape=jax.ShapeDtypeStruct(c_ctx_row.shape, F32),
        in_specs=[VMEM_SPEC] * 2, out_specs=VMEM_SPEC, compiler_params=_cparams(),
    )(gathered, c_ctx_row)


def _relayout_w_in(w):
    pad = jnp.zeros((w.shape[0], 128 - 2 * RANK), w.dtype)
    return jnp.concatenate([w[:, :9 * HW], w[:, 9 * HW + 2 * RANK:], w[:, 9 * HW:9 * HW + 2 * RANK], pad], axis=1)


def _w_in_grad_blocks(g_main, g_lr, n_blocks):
    lr0 = 9 * HW
    n = (g_main.shape[1] + 2 * RANK) // n_blocks

    def cols(lo, hi):
        out = []
        if lo < lr0:
            out.append(g_main[:, lo:min(hi, lr0)])
        if hi > lr0 and lo < lr0 + 2 * RANK:
            out.append(g_lr[:, max(lo, lr0) - lr0:min(hi, lr0 + 2 * RANK) - lr0])
        if hi > lr0 + 2 * RANK:
            out.append(g_main[:, max(lo, lr0 + 2 * RANK) - 2 * RANK:hi - 2 * RANK])
        return out

    return jnp.stack([jnp.concatenate(cols(j * n, (j + 1) * n), axis=1) for j in range(n_blocks)])


def _blocked(full, n_blocks):
    k, n = full.shape
    return full.reshape(k, n_blocks, n // n_blocks).transpose(1, 0, 2)


def _unblocked(blocks):
    nb, k, n = blocks.shape
    return blocks.transpose(1, 0, 2).reshape(k, nb * n)


def _sample_front(x0, ctx0, modc, modx, norm_pre1, lb_full, gla_side, w_in_r):
    ctx_len = ctx0.shape[0]
    n_ctx_tiles = ctx_len // TM
    n_ctx_chunks = ctx_len // CHUNK
    z = jnp.concatenate([ctx0, x0], axis=0)
    h1, p = _in_projection(z, modc, modx, norm_pre1, w_in_r, n_ctx_tiles, "in_projection")
    o_hg_fw, st_hg_fw = _scan_fwd(p, lb_full, n_ctx_chunks, True, "hg", "scan_hg_fw")
    o_hg_bw, st_hg_bw = _scan_fwd(p, lb_full, n_ctx_chunks, False, "hg", "scan_hg_bw")
    o_gla_fw, st_gla_fw = _scan_fwd(p, gla_side[0], n_ctx_chunks, True, "gla", "scan_gla_fw")
    o_gla_bw, st_gla_bw = _scan_fwd(p, gla_side[1], n_ctx_chunks, False, "gla", "scan_gla_bw")
    return dict(z=z, h1=h1, p=p, o_list=[o_hg_fw, o_hg_bw, o_gla_fw, o_gla_bw],
                states=[st_hg_fw, st_hg_bw, st_gla_fw, st_gla_bw])


def _sample_back(reduce, front, x0, ctx0, target0, modc, modx, norm_pre1, norms, onorms, lb_full, gla_side, w_in_r,
                 wbh, wbg, wout, wg, wu, wd):
    seq, d = x0.shape
    ctx_len = ctx0.shape[0]
    n_ctx_tiles = ctx_len // TM
    n_tiles = (ctx_len + seq) // TM
    n_ctx_chunks = ctx_len // CHUNK
    z, h1, p, o_list = front["z"], front["h1"], front["p"], front["o_list"]
    st_hg_fw, st_hg_bw, st_gla_fw, st_gla_bw = front["states"]
    (loss_part, dz2, y1, merged, og_hg, og_gla, h2, a_act, du, dv, dy2, stat_ffn) = _mixer_ffn(
        x0, p, o_list, modx, norms, onorms, wbh, wbg, wout, wg, wu, wd, target0, n_ctx_tiles, "mixer_ffn")
    dff = wg.shape[1]
    tn_ff = dff // N_CHIP if (dff // N_CHIP) % 128 == 0 else 256
    tok = reduce("ffn", [_weight_grad(h2, du, "grad_w_ff_gate", tn=tn_ff),
                         _weight_grad(h2, dv, "grad_w_ff_up", tn=tn_ff),
                         _weight_grad(a_act, dy2, "grad_w_ff_down", tk=tn_ff)])

    (d_ohg, d_ogla, d_hgate, d_ggate, d_ghg, d_ggla, dy1, db_hg, db_gla, stat_mix) = _mixer_tail_bwd(
        x0, p, o_list, dz2, y1, modx + tok, norms, onorms, wbh, wbg, wout, n_ctx_tiles, n_tiles, "mixer_tail_bwd")
    tok = reduce("mix", [_weight_grad(og_hg, db_hg, "grad_w_br_hg"), _weight_grad(og_gla, db_gla, "grad_w_br_gla"),
                         _weight_grad(merged, dy1, "grad_w_out")])
    lb_b = lb_full + tok
    gla_b = [(wgk, bias + tok) for wgk, bias in gla_side]
    dhq_f, dhi_f, dhf_f, dlb_f = _scan_bwd(p, lb_b, st_hg_fw, d_ohg, n_ctx_chunks, True, "hg", "scan_hg_fw_bwd")
    dhq_b, dhi_b, dhf_b, dlb_b = _scan_bwd(p, lb_b, st_hg_bw, d_ohg, n_ctx_chunks, False, "hg", "scan_hg_bw_bwd")
    dgq_f, dgk_f, dgv_f, dlr_f, dwgk_f, dbgk_f = _scan_bwd(p, gla_b[0], st_gla_fw, d_ogla, n_ctx_chunks, True, "gla",
                                                           "scan_gla_fw_bwd")
    dgq_b, dgk_b, dgv_b, dlr_b, dwgk_b, dbgk_b = _scan_bwd(p, gla_b[1], st_gla_bw, d_ogla, n_ctx_chunks, False, "gla",
                                                           "scan_gla_bw_bwd")
    pieces = [dhq_f, dhq_b, dhi_f, dhi_b, dhf_f, dhf_b, d_hgate, dgq_f, dgq_b, dgk_f, dgk_b, dgv_f, dgv_b, d_ggate,
              d_ghg, d_ggla, dlr_f, dlr_b]
    dp, grad_x, stat_in = _in_projection_bwd(z, dz2, modc, modx, norm_pre1, w_in_r, pieces, n_ctx_tiles,
                                             "in_projection_bwd")

    g_in_main = _weight_grad_cols(h1, dp, OFF_LR, "grad_w_in_main")
    g_in_lr = _weight_grad(h1, dp, "grad_w_in_lr", col_block=(128, OFF_LR // 128))
    reduce("in", [_w_in_grad_blocks(g_in_main, g_in_lr, N_CHIP)])
    return dict(
        loss_part=loss_part, grad_x=grad_x, stat_in=stat_in, stat_mix=stat_mix, stat_ffn=stat_ffn,
        dlb=(dlb_f, dlb_b), dwgk=(dwgk_f, dwgk_b), dbgk=(dbgk_f, dbgk_b))


def kernel(x, c, ctx, c_ctx, w_mod, b_mod, norm_pre1, norm_post1, norm_pre2, norm_post2, w_in, hg_lb, hg_onorm, gla_w_gk, gla_b_gk, gla_onorm, w_br_hg, w_br_gla, w_out, w_ff_gate, w_ff_up, w_ff_down, loss_target, m_c_ctx, m_w_mod, m_b_mod, m_norm_pre1, m_norm_post1, m_norm_pre2, m_norm_post2, m_w_in, m_hg_lb, m_hg_onorm, m_gla_w_gk, m_gla_b_gk, m_gla_onorm, m_w_br_hg, m_w_br_gla, m_w_out, m_w_ff_gate, m_w_ff_up, m_w_ff_down, v_c_ctx, v_w_mod, v_b_mod, v_norm_pre1, v_norm_post1, v_norm_pre2, v_norm_post2, v_w_in, v_hg_lb, v_hg_onorm, v_gla_w_gk, v_gla_b_gk, v_gla_onorm, v_w_br_hg, v_w_br_gla, v_w_out, v_w_ff_gate, v_w_ff_up, v_w_ff_down):
    seq, d = x.shape[1], x.shape[2]
    ctx_len = ctx.shape[1]
    assert seq % TM == 0 and ctx_len % TM == 0 and d == 2 * HW
    ax, ay, ac = lax.axis_index("x"), lax.axis_index("y"), lax.axis_index("c")
    chip = 2 * ax + ay
    dev = 2 * chip + ac
    c_arr = jnp.reshape(ac, (1,)).astype(jnp.int32)

    nc = d // 128
    pad8 = lambda a: jnp.pad(a, ((0, -a.shape[0] % 8), (0, 0)))
    small1 = jnp.concatenate([c.reshape(nc, 128), pad8(hg_lb.reshape(4, 128)), gla_w_gk.reshape(2 * RANK, 128),
                              pad8(gla_b_gk.reshape(2, 128))], axis=0)
    got1 = _allgather8(small1, "gather_small_params")
    c_all = got1[:, :nc, :].reshape(N_DEV, d)
    per_chip = got1[0::2]
    lb_full = per_chip[:, nc:nc + 4, :].transpose(1, 0, 2).reshape(4, HW)
    wgk_full = per_chip[:, nc + 8:nc + 8 + 2 * RANK, :].transpose(1, 0, 2).reshape(2, RANK, HW)
    bgk_full = per_chip[:, nc + 8 + 2 * RANK:nc + 10 + 2 * RANK, :].transpose(1, 0, 2).reshape(2, HW)
    wgk_pad = [jnp.zeros((128, HW), F32).at[dd * RANK:(dd + 1) * RANK].set(wgk_full[dd]) for dd in range(2)]
    bgk = [bgk_full[dd:dd + 1] for dd in range(2)]

    n_mod_cols = w_mod.shape[2]
    cond = jnp.concatenate([c_all, pad8(c_ctx.reshape(1, d))], axis=0)
    b_cols = lax.dynamic_slice(b_mod, (0, chip * n_mod_cols), (1, n_mod_cols))
    mod_part = _mod_forward(cond, w_mod[0], b_cols, "mod_forward")
    mod_got = _allgather8(mod_part, "gather_mod")
    mod_all = mod_got[0::2].transpose(1, 0, 2).reshape(16, N_CHIP * n_mod_cols)
    modx = pad8(lax.dynamic_slice(mod_all, (dev, 0), (1, N_MOD * d)).reshape(N_MOD, d))
    modc = pad8(mod_all[8].reshape(N_MOD, d))

    chip_arr = jnp.reshape(chip, (1,)).astype(jnp.int32)
    blocks = [_cast_into_blocks(chip_arr, w_[0], "cast_" + nm) for w_, nm in (
        (w_in, "w_in"), (w_br_hg, "w_br_hg"), (w_br_gla, "w_br_gla"), (w_out, "w_out"), (w_ff_gate, "w_ff_gate"),
        (w_ff_up, "w_ff_up"), (w_ff_down, "w_ff_down"))]
    gathered_in = _gather_blocks(blocks[:1], "gather_w_in")
    sems, lands, token = _blocks_start(blocks[1:], "gather_rest_start")
    w_in_r = _relayout_w_in(_unblocked(gathered_in[0]))

    norms = jnp.concatenate([norm_pre1, norm_post1, norm_pre2, norm_post2, jnp.zeros((4, d), F32)], axis=0)
    onorms = jnp.zeros((8, d), F32).at[0, :HD].set(hg_onorm[0]).at[1, :HD].set(gla_onorm[0])
    gla_side = [(wgk_pad[dd], bgk[dd]) for dd in range(2)]
    modx = modx + token[0, 0]
    front = _sample_front(x[0], ctx[0], modc, modx, norm_pre1, lb_full, gla_side, w_in_r)
    lands = _blocks_wait(sems, lands, front["o_list"][3], "gather_rest_wait")
    gathered = _blocks_finish(lands, "gather_rest_finish")
    wbh, wbg = _unblocked(gathered[0]), _unblocked(gathered[1])
    wout = gathered[2].reshape(d, d)
    wg, wu = _unblocked(gathered[3]), _unblocked(gathered[4])
    wd = gathered[5].reshape(wg.shape[1], d)
    dff = wg.shape[1]
    groups = {"ffn": ["w_ff_gate", "w_ff_up", "w_ff_down"], "mix": ["w_br_hg", "w_br_gla", "w_out"], "in": ["w_in"]}
    row_sharded = {"w_out": d // N_CHIP, "w_ff_down": dff // N_CHIP}
    in_flight = {}

    def reduce(group, grads):
        nms = groups[group]
        full = [g if g.ndim == 3 else g.reshape(N_CHIP, row_sharded[nm], d) if nm in row_sharded
                else _blocked(g, N_CHIP) for g, nm in zip(grads, nms)]
        from_sibling = _send_other_half(full, "grads_to_sibling_" + group)
        pairs = [_pair_sum(c_arr, f, r_, "pair_sum_" + nm) for f, r_, nm in zip(full, from_sibling, nms)]
        sems_, pairs, lands_, token_ = _scatter_start(pairs, "grads_to_owner_start_" + group)
        in_flight[group] = (sems_, pairs, lands_, token_)
        return token_[0, 0]

    r = _sample_back(reduce, front, x[0], ctx[0], loss_target[0], modc, modx, norm_pre1, norms, onorms, lb_full,
                     gla_side, w_in_r, wbh, wbg, wout, wg, wu, wd)
    loss_part, grad_x, stat_in, stat_mix, stat_ffn = (r[k] for k in ("loss_part", "grad_x", "stat_in", "stat_mix",
                                                                     "stat_ffn"))
    (dlb_f, dlb_b), (dwgk_f, dwgk_b), (dbgk_f, dbgk_b) = r["dlb"], r["dwgk"], r["dbgk"]

    weights = dict(w_in=(w_in, m_w_in, v_w_in), w_br_hg=(w_br_hg, m_w_br_hg, v_w_br_hg),
                   w_br_gla=(w_br_gla, m_w_br_gla, v_w_br_gla), w_out=(w_out, m_w_out, v_w_out),
                   w_ff_gate=(w_ff_gate, m_w_ff_gate, v_w_ff_gate), w_ff_up=(w_ff_up, m_w_ff_up, v_w_ff_up),
                   w_ff_down=(w_ff_down, m_w_ff_down, v_w_ff_down))
    names = ["w_in", "w_br_hg", "w_br_gla", "w_out", "w_ff_gate", "w_ff_up", "w_ff_down"]
    big = {}

    def finish(group, after):
        sems_, pairs, lands_, _ = in_flight[group]
        pairs, lands_ = _scatter_wait(sems_, pairs, lands_, after, "grads_to_owner_wait_" + group)
        own_half = [_sum_owner(chip_arr, pr, g, "chip_sum_" + nm) for pr, g, nm in zip(pairs, lands_, groups[group])]
        other_half = _swap_with_sibling(own_half, "halves_to_sibling_" + group)
        for nm, own, oth in zip(groups[group], own_half, other_half):
            w_, m_, v_ = weights[nm]
            res = _adamw_halves(c_arr, own, oth, w_[0], m_[0], v_[0], "adamw_" + nm)
            big[nm] = [r_[None] for r_ in res]
        return big[groups[group][-1]][1]

    token_in = in_flight["in"][3]
    done_ffn = finish("ffn", [token_in])
    done_mix = finish("mix", [done_ffn])

    small2 = jnp.concatenate([
        stat_in + token_in[0, 0], stat_mix, stat_ffn,
        jnp.concatenate([dlb_f, dlb_b], axis=1), jnp.concatenate([dbgk_f, dbgk_b], axis=1),
        jnp.concatenate([dwgk_f[0:RANK], dwgk_b[RANK:2 * RANK]], axis=1)], axis=0)
    assert small2.shape[0] == SMALL_ROWS
    got2 = _allgather8(small2, "gather_small_grads")
    total, dmod_all, g_b_mod, g_lb_full = _reduce_small(got2, lb_full, "reduce_small")
    dmod_cols = lax.dynamic_slice(dmod_all, (0, chip * n_mod_cols), (16, n_mod_cols))
    g_w_mod, cctx_part = _mod_backward(cond, w_mod[0], dmod_cols, "mod_backward")
    got3 = _allgather8(cctx_part, "gather_c_ctx_grad")
    g_c_ctx = _c_ctx_grad(got3, c_ctx.reshape(1, d), "c_ctx_grad")

    g_pre1, g_post1, g_pre2, g_post2 = (total[r_:r_ + 1] for r_ in (ROW_PRE1, ROW_POST1, ROW_PRE2, ROW_POST2))
    g_hg_on, g_gla_on = total[ROW_ONORM:ROW_ONORM + 1, 0:HD], total[ROW_ONORM:ROW_ONORM + 1, HD:2 * HD]
    n_lb = hg_lb.shape[2]
    g_hg_lb = lax.dynamic_slice(g_lb_full, (0, chip * n_lb), (4, n_lb))
    g_bgk = lax.dynamic_slice(total[ROW_BGK:ROW_BGK + 1].reshape(2, HW), (0, chip * n_lb), (2, n_lb))
    g_wgk_full = total[ROW_WGK:ROW_WGK + RANK].reshape(RANK, 2, HW).transpose(1, 0, 2).reshape(2 * RANK, HW)
    g_wgk = lax.dynamic_slice(g_wgk_full, (0, chip * n_lb), (2 * RANK, n_lb))

    small_items = [
        (g_c_ctx, c_ctx.reshape(1, d), m_c_ctx.reshape(1, d), v_c_ctx.reshape(1, d)),
        (g_b_mod, b_mod, m_b_mod, v_b_mod),
        (g_pre1, norm_pre1, m_norm_pre1, v_norm_pre1),
        (g_post1, norm_post1, m_norm_post1, v_norm_post1),
        (g_pre2, norm_pre2, m_norm_pre2, v_norm_pre2),
        (g_post2, norm_post2, m_norm_post2, v_norm_post2),
        (g_hg_lb, hg_lb.reshape(4, n_lb), m_hg_lb.reshape(4, n_lb), v_hg_lb.reshape(4, n_lb)),
        (g_hg_on, hg_onorm, m_hg_onorm, v_hg_onorm),
        (g_wgk, gla_w_gk.reshape(2 * RANK, n_lb), m_gla_w_gk.reshape(2 * RANK, n_lb), v_gla_w_gk.reshape(2 * RANK, n_lb)),
        (g_bgk, gla_b_gk.reshape(2, n_lb), m_gla_b_gk.reshape(2, n_lb), v_gla_b_gk.reshape(2, n_lb)),
        (g_gla_on, gla_onorm, m_gla_onorm, v_gla_onorm),
    ]
    small_res = _adamw_whole(small_items, "adamw_small")
    mod_res = _adamw_tiled(g_w_mod, w_mod[0], m_w_mod[0], v_w_mod[0], "adamw_w_mod")
    finish("in", [done_mix, mod_res[0], small_res[0][0]])

    loss = lax.psum(loss_part[0, 0], ("x", "y", "c"))

    shapes = dict(c_ctx=c_ctx.shape, b_mod=b_mod.shape, norm_pre1=norm_pre1.shape, norm_post1=norm_post1.shape,
                  norm_pre2=norm_pre2.shape, norm_post2=norm_post2.shape, hg_lb=hg_lb.shape, hg_onorm=hg_onorm.shape,
                  gla_w_gk=gla_w_gk.shape, gla_b_gk=gla_b_gk.shape, gla_onorm=gla_onorm.shape)
    small_names = ["c_ctx", "b_mod", "norm_pre1", "norm_post1", "norm_pre2", "norm_post2", "hg_lb", "hg_onorm",
                   "gla_w_gk", "gla_b_gk", "gla_onorm"]
    grads, deltas, new_m, new_v = {}, {}, {}, {}
    for nm, item, res in zip(small_names, small_items, small_res):
        grads[nm] = item[0].reshape(shapes[nm])
        deltas[nm], new_m[nm], new_v[nm] = (r.reshape(shapes[nm]) for r in res)
    grads["w_mod"] = g_w_mod[None]
    deltas["w_mod"], new_m["w_mod"], new_v["w_mod"] = (r[None] for r in mod_res)
    for nm in names:
        grads[nm], deltas[nm], new_m[nm], new_v[nm] = big[nm]
    order = ["c_ctx", "w_mod", "b_mod", "norm_pre1", "norm_post1", "norm_pre2", "norm_post2", "w_in", "hg_lb",
             "hg_onorm", "gla_w_gk", "gla_b_gk", "gla_onorm", "w_br_hg", "w_br_gla", "w_out", "w_ff_gate", "w_ff_up",
             "w_ff_down"]
    return (loss, grad_x[None], *[grads[n] for n in order], *[deltas[n] for n in order],
            *[new_m[n] for n in order], *[new_v[n] for n in order])


def _weight_grad_cols(xs, dy, n_cols, name, tk=256, tn=512):
    rows = dy.shape[0]
    k = xs.shape[1]

    def body(x_ref, dy_ref, o_ref):
        o_ref[...] = _dot_tn(x_ref[...], dy_ref[...])

    return pl.pallas_call(
        body, name=name, grid=(k // tk, n_cols // tn),
        out_shape=jax.ShapeDtypeStruct((k, n_cols), F32),
        in_specs=[pl.BlockSpec((rows, tk), lambda i, j: (0, i)), pl.BlockSpec((rows, tn), lambda i, j: (0, j))],
        out_specs=pl.BlockSpec((tk, tn), lambda i, j: (i, j)),
        compiler_params=_cparams(dimension_semantics=("parallel", "parallel")),
    )(xs, dy)
```

```python
import functools

import jax
import jax.numpy as jnp
from jax import lax
from jax.experimental import pallas as pl
from jax.experimental.pallas import tpu as pltpu

F32 = jnp.float32
BF16 = jnp.bfloat16
HIGHEST = lax.Precision.HIGHEST
MESH = pl.DeviceIdType.MESH

EPS = 1e-6
CHUNK = 64
SUB = 16
NSUB = CHUNK // SUB
NH = 4
HD = 128
HW = NH * HD
RANK = 16
GATE_NORM = 16.0
N_MOD = 6
TM = 256
TM_FFN = 128
N_DEV = 8
N_CHIP = 4
VMEM_LIMIT = 56 * 1024 * 1024

ADAM_LR = 0.001
ADAM_B1 = 0.9
ADAM_B2 = 0.999
ADAM_EPS = 1e-08
ADAM_WD = 0.01
ADAM_STEP = 10

VMEM_SPEC = pl.BlockSpec(memory_space=pltpu.VMEM)
ANY_SPEC = pl.BlockSpec(memory_space=pl.ANY)
HBM_SPEC = pl.BlockSpec(memory_space=pltpu.HBM)
SEM_SPEC = pl.BlockSpec(memory_space=pltpu.SEMAPHORE)
EFFECT = pltpu.SideEffectType.DATAFLOW_SIDE_EFFECTING


def _cparams(**kw):
    return pltpu.CompilerParams(vmem_limit_bytes=VMEM_LIMIT, **kw)


def _dot(a, b):
    return jnp.dot(a.astype(BF16), b.astype(BF16), preferred_element_type=F32)


def _dot_nt(a, b):
    return lax.dot_general(a.astype(BF16), b.astype(BF16), (((1,), (1,)), ((), ())), preferred_element_type=F32)


def _dot_tn(a, b):
    return lax.dot_general(a.astype(BF16), b.astype(BF16), (((0,), (0,)), ((), ())), preferred_element_type=F32)


def _sigmoid(x):
    return 1.0 / (1.0 + jnp.exp(-x))


def _silu(x):
    return x * _sigmoid(x)


def _dsilu(x):
    s = _sigmoid(x)
    return s * (1.0 + x * (1.0 - s))


def _log_sigmoid(x):
    return jnp.minimum(x, 0.0) - jnp.log(1.0 + jnp.exp(-jnp.abs(x)))


def _colsum(a):
    return jnp.sum(a, axis=0, keepdims=True)


def _rms(a):
    r = lax.rsqrt(jnp.mean(a * a, axis=-1, keepdims=True) + EPS)
    return a * r, r


def _rms_bwd(dn, n, r):
    return r * (dn - n * jnp.mean(dn * n, axis=-1, keepdims=True))


def _place():
    x, y, c = lax.axis_index("x"), lax.axis_index("y"), lax.axis_index("c")
    chips = [(1 - x, y), (x, 1 - y), (1 - x, 1 - y)]
    return x, y, c, chips


def _allgather8(v, name):
    rows, cols = v.shape

    def body(x_ref, out_ref, send_sems, recv_sems, local_sem):
        x, y, c, chips = _place()
        me, sibling = (x, y, c), (x, y, 1 - c)

        def blk(px, py, pc):
            return out_ref.at[4 * px + 2 * py + pc]

        def copy(k, block, to, src=None):
            return pltpu.make_async_remote_copy(
                src_ref=blk(*block) if src is None else src, dst_ref=blk(*block),
                send_sem=send_sems.at[k], recv_sem=recv_sems.at[k], device_id=to, device_id_type=MESH)

        mine = pltpu.make_async_copy(x_ref, blk(*me), local_sem)
        mine.start()
        first = [copy(0, me, sibling, src=x_ref)]
        first += [copy(1 + j, me, (*chip, c), src=x_ref) for j, chip in enumerate(chips)]
        for cp in first:
            cp.start()
        passed = [copy(4 + j, (*chip, c), sibling) for j, chip in enumerate(chips)]
        for j, chip in enumerate(chips):
            copy(1 + j, (*chip, c), me).wait_recv()
            passed[j].start()
        copy(0, sibling, me).wait_recv()
        for j, chip in enumerate(chips):
            copy(4 + j, (*chip, 1 - c), me).wait_recv()
        for cp in first + passed:
            cp.wait_send()
        mine.wait()

    return pl.pallas_call(
        body, name=name,
        out_shape=jax.ShapeDtypeStruct((N_DEV, rows, cols), v.dtype),
        in_specs=[VMEM_SPEC], out_specs=VMEM_SPEC,
        scratch_shapes=[pltpu.SemaphoreType.DMA((7,)), pltpu.SemaphoreType.DMA((7,)), pltpu.SemaphoreType.DMA],
    )(v)


def _cast_into_blocks(chip_arr, w, name):
    rows, cols = w.shape
    tr = _row_tile(rows, 16, 256)

    def body(chip_ref, w_ref, o_ref):
        o_ref[0] = w_ref[...].astype(BF16)

    return pl.pallas_call(
        body, name=name,
        grid_spec=pltpu.PrefetchScalarGridSpec(
            num_scalar_prefetch=1, grid=(rows // tr,),
            in_specs=[pl.BlockSpec((tr, cols), lambda i, chip_ref: (i, 0))],
            out_specs=pl.BlockSpec((1, tr, cols), lambda i, chip_ref: (chip_ref[0], i, 0))),
        out_shape=jax.ShapeDtypeStruct((N_CHIP, rows, cols), BF16),
        compiler_params=_cparams(dimension_semantics=("parallel",)),
    )(chip_arr, w)


def _half_rows(ref, chip_id, pc):
    h = ref.shape[1] // 2
    return ref.at[chip_id, pl.ds(pl.multiple_of(pc * h, 8), h), :]


def _gather_blocks(lands, name):
    n = len(lands)

    def body(*refs):
        outs = refs[n:2 * n]
        send_sems, recv_sems = refs[2 * n:]
        x, y, c, chips = _place()
        me_chip = 2 * x + y
        sibling = (x, y, 1 - c)

        def copy(k, j, chip_id, pc, to):
            return pltpu.make_async_remote_copy(
                src_ref=_half_rows(outs[k], chip_id, pc), dst_ref=_half_rows(outs[k], chip_id, pc),
                send_sem=send_sems.at[k, j], recv_sem=recv_sems.at[k, j], device_id=to, device_id_type=MESH)

        started = []
        for k in range(n):
            for j, chip in enumerate(chips):
                cp = copy(k, j, me_chip, c, (*chip, c))
                cp.start()
                started.append(cp)
        for k in range(n):
            for j, (px, py) in enumerate(chips):
                copy(k, j, 2 * px + py, c, sibling).wait_recv()
                cp = copy(k, 3 + j, 2 * px + py, c, sibling)
                cp.start()
                started.append(cp)
        for k in range(n):
            for j, (px, py) in enumerate(chips):
                copy(k, 3 + j, 2 * px + py, 1 - c, sibling).wait_recv()
        for cp in started:
            cp.wait_send()

    return pl.pallas_call(
        body, name=name,
        out_shape=[jax.ShapeDtypeStruct(l.shape, l.dtype) for l in lands],
        in_specs=[ANY_SPEC] * n, out_specs=[ANY_SPEC] * n,
        input_output_aliases={i: i for i in range(n)},
        scratch_shapes=[pltpu.SemaphoreType.DMA((n, 6)), pltpu.SemaphoreType.DMA((n, 6))],
    )(*lands)


def _hbm(a):
    return pltpu.with_memory_space_constraint(a, pltpu.HBM)


def _blocks_start(lands, name):
    n = len(lands)
    n_sem = 3 * n

    def body(*refs):
        lnd = refs[:n]
        send_sems, recv_sems = refs[n:n + n_sem], refs[n + n_sem:n + 2 * n_sem]
        token = refs[-1]
        x, y, c, chips = _place()
        me_chip = 2 * x + y
        for k in range(n):
            for j, chip in enumerate(chips):
                pltpu.make_async_remote_copy(
                    src_ref=_half_rows(lnd[k], me_chip, c), dst_ref=_half_rows(lnd[k], me_chip, c),
                    send_sem=send_sems[3 * k + j], recv_sem=recv_sems[3 * k + j],
                    device_id=(*chip, c), device_id_type=MESH).start()
        token[...] = jnp.zeros_like(token)

    out = pl.pallas_call(
        body, name=name,
        out_shape=(*[pltpu.SemaphoreType.DMA(())] * (2 * n_sem),
                   *[pltpu.HBM(l.shape, l.dtype) for l in lands],
                   jax.ShapeDtypeStruct((8, 128), F32)),
        in_specs=[HBM_SPEC] * n,
        out_specs=(*[SEM_SPEC] * (2 * n_sem), *[HBM_SPEC] * n, VMEM_SPEC),
        input_output_aliases={i: 2 * n_sem + i for i in range(n)},
        compiler_params=pltpu.CompilerParams(has_side_effects=EFFECT),
    )(*[_hbm(l) for l in lands])
    return list(out[:2 * n_sem]), list(out[2 * n_sem:2 * n_sem + n]), out[-1]


def _blocks_wait(sems, lands, after, name):
    n = len(lands)
    n_sem = 3 * n

    def body(*refs):
        lnd = refs[:n]
        s_sems, r_sems = refs[n:n + n_sem], refs[n + n_sem:n + 2 * n_sem]
        x, y, c, chips = _place()
        me_chip = 2 * x + y
        for k in range(n):
            for j, (px, py) in enumerate(chips):
                cp = pltpu.make_async_remote_copy(
                    src_ref=_half_rows(lnd[k], me_chip, c), dst_ref=_half_rows(lnd[k], 2 * px + py, c),
                    send_sem=s_sems[3 * k + j], recv_sem=r_sems[3 * k + j],
                    device_id=(px, py, c), device_id_type=MESH)
                cp.wait_send()
                cp.wait_recv()

    out = pl.pallas_call(
        body, name=name,
        out_shape=tuple(pltpu.HBM(l.shape, l.dtype) for l in lands),
        in_specs=[HBM_SPEC] * n + [SEM_SPEC] * (2 * n_sem) + [ANY_SPEC],
        out_specs=[HBM_SPEC] * n,
        input_output_aliases={i: i for i in range(n)},
        compiler_params=pltpu.CompilerParams(has_side_effects=EFFECT),
    )(*lands, *sems, after)
    return list(out)


def _blocks_finish(lands, name):
    n = len(lands)

    def body(*refs):
        lnd = refs[n:2 * n]
        send_sems, recv_sems = refs[2 * n:]
        x, y, c, chips = _place()
        sibling = (x, y, 1 - c)

        def copy(k, j, chip_id, pc):
            return pltpu.make_async_remote_copy(
                src_ref=_half_rows(lnd[k], chip_id, pc), dst_ref=_half_rows(lnd[k], chip_id, pc),
                send_sem=send_sems.at[k, j], recv_sem=recv_sems.at[k, j], device_id=sibling, device_id_type=MESH)

        started = []
        for k in range(n):
            for j, (px, py) in enumerate(chips):
                cp = copy(k, j, 2 * px + py, c)
                cp.start()
                started.append(cp)
        for k in range(n):
            for j, (px, py) in enumerate(chips):
                copy(k, j, 2 * px + py, 1 - c).wait_recv()
        for cp in started:
            cp.wait_send()

    out = pl.pallas_call(
        body, name=name,
        out_shape=[jax.ShapeDtypeStruct(l.shape, l.dtype) for l in lands],
        in_specs=[ANY_SPEC] * n, out_specs=[ANY_SPEC] * n,
        input_output_aliases={i: i for i in range(n)},
        scratch_shapes=[pltpu.SemaphoreType.DMA((n, 3)), pltpu.SemaphoreType.DMA((n, 3))],
    )(*lands)
    return list(out)


def _gather_start(shards, name):
    n = len(shards)
    n_sem = 3 * n

    def body(*refs):
        ins, lands = refs[:n], refs[n:2 * n]
        send_sems, recv_sems = refs[2 * n:2 * n + n_sem], refs[2 * n + n_sem:2 * n + 2 * n_sem]
        token = refs[-1]
        x, y, c, chips = _place()
        me_chip = 2 * x + y
        for k in range(n):
            h = shards[k].shape[0] // 2
            rows = pl.ds(pl.multiple_of(c * h, 8), h)
            for j, chip in enumerate(chips):
                pltpu.make_async_remote_copy(
                    src_ref=ins[k].at[rows, :], dst_ref=lands[k].at[me_chip, rows, :],
                    send_sem=send_sems[3 * k + j], recv_sem=recv_sems[3 * k + j],
                    device_id=(*chip, c), device_id_type=MESH).start()
        token[...] = jnp.zeros_like(token)

    lands = [_hbm(lax.empty((N_CHIP,) + s.shape, s.dtype)) for s in shards]
    out = pl.pallas_call(
        body, name=name,
        out_shape=(*[pltpu.SemaphoreType.DMA(())] * (2 * n_sem),
                   *[pltpu.HBM(s.shape, s.dtype) for s in shards],
                   *[pltpu.HBM(l.shape, l.dtype) for l in lands],
                   jax.ShapeDtypeStruct((8, 128), F32)),
        in_specs=[HBM_SPEC] * (2 * n),
        out_specs=(*[SEM_SPEC] * (2 * n_sem), *[HBM_SPEC] * (2 * n), VMEM_SPEC),
        input_output_aliases={i: 2 * n_sem + i for i in range(2 * n)},
        compiler_params=pltpu.CompilerParams(has_side_effects=EFFECT),
    )(*[_hbm(s) for s in shards], *lands)
    sems = list(out[:2 * n_sem])
    return sems, list(out[2 * n_sem:2 * n_sem + n]), list(out[2 * n_sem + n:2 * n_sem + 2 * n]), out[-1]


def _gather_wait(sems, shards, lands, after, name):
    n = len(shards)
    n_sem = 3 * n

    def body(*refs):
        ins, lnd = refs[:n], refs[n:2 * n]
        s_sems, r_sems = refs[2 * n:2 * n + n_sem], refs[2 * n + n_sem:2 * n + 2 * n_sem]
        x, y, c, chips = _place()
        for k in range(n):
            h = shards[k].shape[0] // 2
            rows = pl.ds(pl.multiple_of(c * h, 8), h)
            for j, (px, py) in enumerate(chips):
                cp = pltpu.make_async_remote_copy(
                    src_ref=ins[k].at[rows, :], dst_ref=lnd[k].at[2 * px + py, rows, :],
                    send_sem=s_sems[3 * k + j], recv_sem=r_sems[3 * k + j],
                    device_id=(px, py, c), device_id_type=MESH)
                cp.wait_send()
                cp.wait_recv()

    out = pl.pallas_call(
        body, name=name,
        out_shape=(*[pltpu.HBM(s.shape, s.dtype) for s in shards], *[pltpu.HBM(l.shape, l.dtype) for l in lands]),
        in_specs=[HBM_SPEC] * (2 * n) + [SEM_SPEC] * (2 * n_sem) + [ANY_SPEC],
        out_specs=[HBM_SPEC] * (2 * n),
        input_output_aliases={i: i for i in range(2 * n)},
        compiler_params=pltpu.CompilerParams(has_side_effects=EFFECT),
    )(*shards, *lands, *sems, after)
    return list(out[:n]), list(out[n:])


def _gather_finish(shards, lands, name):
    n = len(shards)

    def body(*refs):
        ins, lnd = refs[:n], refs[2 * n:3 * n]
        send_sems, recv_sems, local_sems = refs[3 * n:]
        x, y, c, chips = _place()
        me_chip = 2 * x + y
        sibling = (x, y, 1 - c)

        def half(k, chip_id, pc):
            h = shards[k].shape[0] // 2
            return lnd[k].at[chip_id, pl.ds(pl.multiple_of(pc * h, 8), h), :]

        def copy(k, j, chip_id, pc):
            return pltpu.make_async_remote_copy(
                src_ref=half(k, chip_id, pc), dst_ref=half(k, chip_id, pc),
                send_sem=send_sems.at[k, j], recv_sem=recv_sems.at[k, j], device_id=sibling, device_id_type=MESH)

        locals_, started = [], []
        for k in range(n):
            cp = pltpu.make_async_copy(ins[k], lnd[k].at[me_chip], local_sems.at[k])
            cp.start()
            locals_.append(cp)
            for j, (px, py) in enumerate(chips):
                cp = copy(k, j, 2 * px + py, c)
                cp.start()
                started.append(cp)
        for k in range(n):
            for j, (px, py) in enumerate(chips):
                copy(k, j, 2 * px + py, 1 - c).wait_recv()
        for cp in started:
            cp.wait_send()
        for cp in locals_:
            cp.wait()

    out = pl.pallas_call(
        body, name=name,
        out_shape=[jax.ShapeDtypeStruct(l.shape, l.dtype) for l in lands],
        in_specs=[ANY_SPEC] * (2 * n), out_specs=[ANY_SPEC] * n,
        input_output_aliases={n + i: i for i in range(n)},
        scratch_shapes=[pltpu.SemaphoreType.DMA((n, 3)), pltpu.SemaphoreType.DMA((n, 3)),
                        pltpu.SemaphoreType.DMA((n,))],
    )(*shards, *lands)
    return list(out)


def _send_other_half(arrs, name):
    n = len(arrs)

    def body(*refs):
        ins, outs = refs[:n], refs[n:2 * n]
        send_sems, recv_sems = refs[2 * n:]
        x, y, c, _ = _place()
        cps = []
        for k in range(n):
            h = arrs[k].shape[1] // 2
            cp = pltpu.make_async_remote_copy(
                src_ref=ins[k].at[:, pl.ds(pl.multiple_of((1 - c) * h, 8), h), :], dst_ref=outs[k],
                send_sem=send_sems.at[k], recv_sem=recv_sems.at[k], device_id=(x, y, 1 - c), device_id_type=MESH)
            cp.start()
            cps.append(cp)
        for cp in cps:
            cp.wait()

    return pl.pallas_call(
        body, name=name,
        out_shape=[jax.ShapeDtypeStruct((a.shape[0], a.shape[1] // 2, a.shape[2]), a.dtype) for a in arrs],
        in_specs=[ANY_SPEC] * n, out_specs=[ANY_SPEC] * n,
        scratch_shapes=[pltpu.SemaphoreType.DMA((n,)), pltpu.SemaphoreType.DMA((n,))],
    )(*arrs)


def _blocks_to_owner(arrs, name):
    n = len(arrs)

    def body(*refs):
        ins, outs = refs[:n], refs[n:2 * n]
        send_sems, recv_sems, local_sems = refs[2 * n:]
        x, y, c, chips = _place()
        me_chip = 2 * x + y
        locals_, started = [], []
        for k in range(n):
            cp = pltpu.make_async_copy(ins[k].at[me_chip], outs[k].at[me_chip], local_sems.at[k])
            cp.start()
            locals_.append(cp)

        def copy(k, j, src_block, dst_slot, to):
            return pltpu.make_async_remote_copy(
                src_ref=ins[k].at[src_block], dst_ref=outs[k].at[dst_slot],
                send_sem=send_sems.at[k, j], recv_sem=recv_sems.at[k, j], device_id=to, device_id_type=MESH)

        for k in range(n):
            for j, (px, py) in enumerate(chips):
                cp = copy(k, j, 2 * px + py, me_chip, (px, py, c))
                cp.start()
                started.append(cp)
        for k in range(n):
            for j, (px, py) in enumerate(chips):
                copy(k, j, me_chip, 2 * px + py, (px, py, c)).wait_recv()
        for cp in started:
            cp.wait_send()
        for cp in locals_:
            cp.wait()

    return pl.pallas_call(
        body, name=name,
        out_shape=[jax.ShapeDtypeStruct(a.shape, a.dtype) for a in arrs],
        in_specs=[ANY_SPEC] * n, out_specs=[ANY_SPEC] * n,
        scratch_shapes=[pltpu.SemaphoreType.DMA((n, 3)), pltpu.SemaphoreType.DMA((n, 3)),
                        pltpu.SemaphoreType.DMA((n,))],
    )(*arrs)


def _scatter_blocks(arrs, name):
    n = len(arrs)

    def body(*refs):
        ins, outs = refs[:n], refs[n:2 * n]
        send_sems, recv_sems = refs[2 * n:]
        x, y, c, chips = _place()
        me_chip = 2 * x + y

        def copy(k, j, src_block, dst_slot, to):
            return pltpu.make_async_remote_copy(
                src_ref=ins[k].at[src_block], dst_ref=outs[k].at[dst_slot],
                send_sem=send_sems.at[k, j], recv_sem=recv_sems.at[k, j], device_id=to, device_id_type=MESH)

        started = []
        for k in range(n):
            for j, (px, py) in enumerate(chips):
                cp = copy(k, j, 2 * px + py, me_chip, (px, py, c))
                cp.start()
                started.append(cp)
        for k in range(n):
            for j, (px, py) in enumerate(chips):
                copy(k, j, me_chip, 2 * px + py, (px, py, c)).wait_recv()
        for cp in started:
            cp.wait_send()

    return pl.pallas_call(
        body, name=name,
        out_shape=[jax.ShapeDtypeStruct(a.shape, a.dtype) for a in arrs],
        in_specs=[ANY_SPEC] * n, out_specs=[ANY_SPEC] * n,
        scratch_shapes=[pltpu.SemaphoreType.DMA((n, 3)), pltpu.SemaphoreType.DMA((n, 3))],
    )(*arrs)


def _scatter_start(arrs, name, after=()):
    n = len(arrs)
    n_sem = 3 * n
    first = 2 * n + len(after)

    def body(*refs):
        ins, lnd = refs[:n], refs[n:2 * n]
        send_sems, recv_sems = refs[first:first + n_sem], refs[first + n_sem:first + 2 * n_sem]
        token = refs[-1]
        x, y, c, chips = _place()
        me_chip = 2 * x + y
        for k in range(n):
            for j, (px, py) in enumerate(chips):
                pltpu.make_async_remote_copy(
                    src_ref=ins[k].at[2 * px + py], dst_ref=lnd[k].at[me_chip],
                    send_sem=send_sems[3 * k + j], recv_sem=recv_sems[3 * k + j],
                    device_id=(px, py, c), device_id_type=MESH).start()
        token[...] = jnp.zeros_like(token)

    lands = [_hbm(lax.empty(a.shape, a.dtype)) for a in arrs]
    out = pl.pallas_call(
        body, name=name,
        out_shape=(*[pltpu.SemaphoreType.DMA(())] * (2 * n_sem),
                   *[pltpu.HBM(a.shape, a.dtype) for a in arrs], *[pltpu.HBM(a.shape, a.dtype) for a in arrs],
                   jax.ShapeDtypeStruct((8, 128), F32)),
        in_specs=[HBM_SPEC] * (2 * n) + [ANY_SPEC] * len(after),
        out_specs=(*[SEM_SPEC] * (2 * n_sem), *[HBM_SPEC] * (2 * n), VMEM_SPEC),
        input_output_aliases={i: 2 * n_sem + i for i in range(2 * n)},
        compiler_params=pltpu.CompilerParams(has_side_effects=EFFECT),
    )(*[_hbm(a) for a in arrs], *lands, *after)
    base = 2 * n_sem
    return list(out[:base]), list(out[base:base + n]), list(out[base + n:base + 2 * n]), out[-1]


def _scatter_wait(sems, arrs, lands, after, name):
    n = len(arrs)
    n_sem = 3 * n

    def body(*refs):
        ins, lnd = refs[:n], refs[n:2 * n]
        s_sems, r_sems = refs[2 * n:2 * n + n_sem], refs[2 * n + n_sem:2 * n + 2 * n_sem]
        x, y, c, chips = _place()
        for k in range(n):
            for j, (px, py) in enumerate(chips):
                cp = pltpu.make_async_remote_copy(
                    src_ref=ins[k].at[2 * px + py], dst_ref=lnd[k].at[2 * px + py],
                    send_sem=s_sems[3 * k + j], recv_sem=r_sems[3 * k + j],
                    device_id=(px, py, c), device_id_type=MESH)
                cp.wait_send()
                cp.wait_recv()

    out = pl.pallas_call(
        body, name=name,
        out_shape=tuple(pltpu.HBM(a.shape, a.dtype) for a in list(arrs) + list(lands)),
        in_specs=[HBM_SPEC] * (2 * n) + [SEM_SPEC] * (2 * n_sem) + [ANY_SPEC] * len(after),
        out_specs=[HBM_SPEC] * (2 * n),
        input_output_aliases={i: i for i in range(2 * n)},
        compiler_params=pltpu.CompilerParams(has_side_effects=EFFECT),
    )(*arrs, *lands, *sems, *after)
    return list(out[:n]), list(out[n:])


def _sum_owner(chip_arr, pairs, got, name):
    nb, h, cols = got.shape
    tr = _row_tile(h, 16, 256)

    def body(chip_ref, own_ref, a_ref, b_ref, c_ref, o_ref):
        o_ref[...] = ((own_ref[0].astype(F32) + a_ref[0].astype(F32)) + b_ref[0].astype(F32)) + c_ref[0].astype(F32)

    def slot(off):
        return pl.BlockSpec((1, tr, cols), lambda i, chip_ref: ((chip_ref[0] + off) % N_CHIP, i, 0))

    return pl.pallas_call(
        body, name=name,
        grid_spec=pltpu.PrefetchScalarGridSpec(
            num_scalar_prefetch=1, grid=(h // tr,),
            in_specs=[slot(0), slot(1), slot(2), slot(3)],
            out_specs=pl.BlockSpec((tr, cols), lambda i, chip_ref: (i, 0))),
        out_shape=jax.ShapeDtypeStruct((h, cols), F32),
        compiler_params=_cparams(dimension_semantics=("parallel",)),
    )(chip_arr, pairs, got, got, got)


def _swap_with_sibling(arrs, name):
    n = len(arrs)

    def body(*refs):
        ins, outs = refs[:n], refs[n:2 * n]
        send_sems, recv_sems = refs[2 * n:]
        x, y, c, _ = _place()
        cps = []
        for k in range(n):
            cp = pltpu.make_async_remote_copy(
                src_ref=ins[k], dst_ref=outs[k], send_sem=send_sems.at[k], recv_sem=recv_sems.at[k],
                device_id=(x, y, 1 - c), device_id_type=MESH)
            cp.start()
            cps.append(cp)
        for cp in cps:
            cp.wait()

    return pl.pallas_call(
        body, name=name,
        out_shape=[jax.ShapeDtypeStruct(a.shape, a.dtype) for a in arrs],
        in_specs=[ANY_SPEC] * n, out_specs=[ANY_SPEC] * n,
        scratch_shapes=[pltpu.SemaphoreType.DMA((n,)), pltpu.SemaphoreType.DMA((n,))],
    )(*arrs)


def _row_tile(h, mult=8, cap=128):
    for t in range(cap - cap % mult, mult - 1, -mult):
        if h % t == 0:
            return t
    raise ValueError(h)


def _cast_bf16(a, name):
    rows, cols = a.shape
    tr = _row_tile(rows, 16, 256)

    def body(a_ref, o_ref):
        o_ref[...] = a_ref[...].astype(BF16)

    return pl.pallas_call(
        body, name=name, grid=(rows // tr,),
        out_shape=jax.ShapeDtypeStruct(a.shape, BF16),
        in_specs=[pl.BlockSpec((tr, cols), lambda i: (i, 0))],
        out_specs=pl.BlockSpec((tr, cols), lambda i: (i, 0)),
        compiler_params=_cparams(dimension_semantics=("parallel",)),
    )(a)


def _pair_sum(c_arr, full, recv, name):
    nb, rows, cols = full.shape
    h = rows // 2
    tr = _row_tile(h, 16, 256)
    steps = h // tr

    def body(c_ref, f_ref, r_ref, o_ref):
        o_ref[...] = (f_ref[...] + r_ref[...]).astype(BF16)

    return pl.pallas_call(
        body, name=name,
        grid_spec=pltpu.PrefetchScalarGridSpec(
            num_scalar_prefetch=1, grid=(nb, steps),
            in_specs=[pl.BlockSpec((1, tr, cols), lambda b, i, c_ref: (b, c_ref[0] * steps + i, 0)),
                      pl.BlockSpec((1, tr, cols), lambda b, i, c_ref: (b, i, 0))],
            out_specs=pl.BlockSpec((1, tr, cols), lambda b, i, c_ref: (b, i, 0))),
        out_shape=jax.ShapeDtypeStruct((nb, h, cols), BF16),
        compiler_params=_cparams(dimension_semantics=("parallel", "parallel")),
    )(c_arr, full, recv)


def _sum_chips(got, name):
    nb, h, cols = got.shape
    tr = _row_tile(h, 16, 256)

    def body(g_ref, o_ref):
        g = g_ref[...].astype(F32)
        o_ref[...] = ((g[0] + g[1]) + g[2]) + g[3]

    return pl.pallas_call(
        body, name=name, grid=(h // tr,),
        out_shape=jax.ShapeDtypeStruct((h, cols), F32),
        in_specs=[pl.BlockSpec((nb, tr, cols), lambda i: (0, i, 0))],
        out_specs=pl.BlockSpec((tr, cols), lambda i: (i, 0)),
        compiler_params=_cparams(dimension_semantics=("parallel",)),
    )(got)


def _adam_math(g, w, m, v):
    m1 = ADAM_B1 * m + (1.0 - ADAM_B1) * g
    v1 = ADAM_B2 * v + (1.0 - ADAM_B2) * (g * g)
    m_hat = m1 / (1.0 - ADAM_B1 ** ADAM_STEP)
    v_hat = v1 / (1.0 - ADAM_B2 ** ADAM_STEP)
    delta = -ADAM_LR * (m_hat / (jnp.sqrt(v_hat) + ADAM_EPS) + ADAM_WD * w)
    return delta, m1, v1


def _adamw_halves(c_arr, own, other, w, m, v, name):
    rows, cols = w.shape
    h = rows // 2
    tr = _row_tile(h)
    steps = h // tr

    def body(c_ref, own_ref, oth_ref, w_ref, m_ref, v_ref, g_out, d_out, m_out, v_out):
        g = jnp.where(pl.program_id(0) == c_ref[0], own_ref[...], oth_ref[...])
        d, m1, v1 = _adam_math(g, w_ref[...], m_ref[...], v_ref[...])
        g_out[...] = g
        d_out[...] = d
        m_out[...] = m1
        v_out[...] = v1

    half_spec = pl.BlockSpec((tr, cols), lambda p, i, c_ref: (i, 0))
    full_spec = pl.BlockSpec((tr, cols), lambda p, i, c_ref: (p * steps + i, 0))
    return pl.pallas_call(
        body, name=name,
        grid_spec=pltpu.PrefetchScalarGridSpec(
            num_scalar_prefetch=1, grid=(2, steps),
            in_specs=[half_spec, half_spec, full_spec, full_spec, full_spec],
            out_specs=[full_spec] * 4),
        out_shape=[jax.ShapeDtypeStruct(w.shape, F32)] * 4,
        compiler_params=_cparams(dimension_semantics=("parallel", "parallel")),
    )(c_arr, own, other, w, m, v)


def _adamw_whole(items, name):
    n = len(items)

    def body(*refs):
        ins, outs = refs[:4 * n], refs[4 * n:]
        for k in range(n):
            g, w, m, v = (r[...] for r in ins[4 * k:4 * k + 4])
            d, m1, v1 = _adam_math(g, w, m, v)
            outs[3 * k][...] = d
            outs[3 * k + 1][...] = m1
            outs[3 * k + 2][...] = v1

    flat = [a for it in items for a in it]
    shapes = [jax.ShapeDtypeStruct(it[1].shape, F32) for it in items for _ in range(3)]
    out = pl.pallas_call(
        body, name=name, out_shape=shapes,
        in_specs=[VMEM_SPEC] * (4 * n), out_specs=[VMEM_SPEC] * (3 * n),
        compiler_params=_cparams(),
    )(*flat)
    return [tuple(out[3 * k:3 * k + 3]) for k in range(n)]


def _adamw_tiled(g, w, m, v, name):
    rows, cols = w.shape
    tr = _row_tile(rows)

    def body(g_ref, w_ref, m_ref, v_ref, d_out, m_out, v_out):
        d, m1, v1 = _adam_math(g_ref[...], w_ref[...], m_ref[...], v_ref[...])
        d_out[...] = d
        m_out[...] = m1
        v_out[...] = v1

    spec = pl.BlockSpec((tr, cols), lambda i: (i, 0))
    return pl.pallas_call(
        body, name=name, grid=(rows // tr,),
        out_shape=[jax.ShapeDtypeStruct(w.shape, F32)] * 3,
        in_specs=[spec] * 4, out_specs=[spec] * 3,
        compiler_params=_cparams(dimension_semantics=("parallel",)),
    )(g, w, m, v)


def _mod_forward(cond, w_mod, b_mod_cols, name):
    def body(c_ref, w_ref, b_ref, o_ref):
        o_ref[...] = _dot(_silu(c_ref[...]), w_ref[...]) + b_ref[...]

    return pl.pallas_call(
        body, name=name, out_shape=jax.ShapeDtypeStruct((cond.shape[0], w_mod.shape[1]), F32),
        in_specs=[VMEM_SPEC] * 3, out_specs=VMEM_SPEC, compiler_params=_cparams(),
    )(cond, w_mod, b_mod_cols)


def _mod_backward(cond, w_mod, dmod_cols, name):
    def body(c_ref, w_ref, d_ref, gw_ref, gc_ref):
        s = _silu(c_ref[...])
        d = d_ref[...]
        gw_ref[...] = _dot_tn(s, d)
        gc_ref[...] = _dot_nt(d[8:16, :], w_ref[...])

    return pl.pallas_call(
        body, name=name,
        out_shape=[jax.ShapeDtypeStruct(w_mod.shape, F32), jax.ShapeDtypeStruct((8, w_mod.shape[0]), F32)],
        in_specs=[VMEM_SPEC] * 3, out_specs=[VMEM_SPEC] * 2, compiler_params=_cparams(),
    )(cond, w_mod, dmod_cols)


def _col_chunks(width, step=512):
    return [(s, min(step, width - s)) for s in range(0, width, step)]


def _in_projection(z, modc, modx, pre1, w_r, n_ctx_tiles, name):
    rows, d = z.shape
    width = w_r.shape[1]

    def body(z_ref, modc_ref, modx_ref, pre_ref, w_ref, h_ref, p_ref):
        is_ctx = pl.program_id(0) < n_ctx_tiles
        n, _ = _rms(z_ref[...])
        shift = jnp.where(is_ctx, modc_ref[0:1, :], modx_ref[0:1, :])
        scale = jnp.where(is_ctx, modc_ref[1:2, :], modx_ref[1:2, :])
        h = (n * pre_ref[...] * (1.0 + scale) + shift).astype(BF16)
        h_ref[...] = h
        for s, w in _col_chunks(width):
            p_ref[:, s:s + w] = jnp.dot(h, w_ref[:, s:s + w], preferred_element_type=F32)

    row = lambda i: (i, 0)
    fixed = lambda i: (0, 0)
    return pl.pallas_call(
        body, name=name, grid=(rows // TM,),
        out_shape=[jax.ShapeDtypeStruct((rows, d), BF16), jax.ShapeDtypeStruct((rows, width), F32)],
        in_specs=[pl.BlockSpec((TM, d), row), pl.BlockSpec((8, d), fixed), pl.BlockSpec((8, d), fixed),
                  pl.BlockSpec((1, d), fixed), VMEM_SPEC],
        out_specs=[pl.BlockSpec((TM, d), row), pl.BlockSpec((TM, width), row)],
        compiler_params=_cparams(dimension_semantics=("parallel",)),
    )(z, modc, modx, pre1, w_r)


C_HQ, C_HI, C_HF_FW, C_HF_BW, C_HGATE, C_GQ, C_GK, C_GV, C_GGATE = range(9)
OFF_GATE_HG = 9 * HW
OFF_LR = 13 * HW
P_WIDTH = OFF_LR + 128


def _head_norm_fwd(o, w):
    outs, ns, rs = [], [], []
    for h in range(NH):
        n, r = _rms(o[:, h * HD:(h + 1) * HD])
        ns.append(n)
        rs.append(r)
        outs.append(n * w)
    return jnp.concatenate(outs, axis=1), ns, rs


def _mixer_tail(z, o_hg, o_gla, p_hgate, p_ggate, p_gate_hg, p_gate_gla, hg_on, gla_on, wbh, wbg, wout):
    on_hg, n_hg, r_hg = _head_norm_fwd(o_hg, hg_on)
    on_gla, n_gla, r_gla = _head_norm_fwd(o_gla, gla_on)
    og_hg = (on_hg * _silu(p_hgate)).astype(BF16)
    og_gla = (on_gla * _silu(p_ggate)).astype(BF16)
    b_hg = jnp.dot(og_hg, wbh, preferred_element_type=F32)
    b_gla = jnp.dot(og_gla, wbg, preferred_element_type=F32)
    s_hg = _sigmoid(p_gate_hg)
    s_gla = _sigmoid(p_gate_gla)
    merged = (s_hg * b_hg + s_gla * b_gla).astype(BF16)
    y1 = jnp.dot(merged, wout, preferred_element_type=F32)
    return dict(on_hg=on_hg, n_hg=n_hg, r_hg=r_hg, on_gla=on_gla, n_gla=n_gla, r_gla=r_gla, og_hg=og_hg,
                og_gla=og_gla, b_hg=b_hg, b_gla=b_gla, s_hg=s_hg, s_gla=s_gla, merged=merged, y1=y1)


def _mixer_ffn(x_lat, p, o_list, modx, norms, onorms, w_br_hg, w_br_gla, w_out, w_gate, w_up, w_down, target,
               n_ctx_tiles, name):
    rows, d = x_lat.shape
    dff = w_gate.shape[1]
    inv_d = 1.0 / d

    def body(x_ref, ofw_hg, obw_hg, ofw_gla, obw_gla, p_hgate, p_ggate, p_ghg_a, p_ghg_b, p_ggla_a, p_ggla_b,
             modx_ref, norm_ref, on_ref, wbh_ref, wbg_ref, wout_ref, wg_ref, wu_ref, wd_ref, t_ref,
             loss_ref, dz2_ref, y1_ref, mrg_ref, oghg_ref, oggla_ref, h2_ref, a_ref, du_ref, dv_ref, dy2_ref,
             stat_ref):
        i = pl.program_id(0)
        post1, pre2, post2 = norm_ref[1:2, :], norm_ref[2:3, :], norm_ref[3:4, :]
        gate1, shift2, scale2, gate2 = modx_ref[2:3, :], modx_ref[3:4, :], modx_ref[4:5, :], modx_ref[5:6, :]
        p_gate_hg = jnp.concatenate([p_ghg_a[...], p_ghg_b[...]], axis=1)
        p_gate_gla = jnp.concatenate([p_ggla_a[...], p_ggla_b[...]], axis=1)
        t = _mixer_tail(x_ref[...], ofw_hg[...] + obw_hg[...], ofw_gla[...] + obw_gla[...], p_hgate[...],
                        p_ggate[...], p_gate_hg, p_gate_gla, on_ref[0:1, 0:HD], on_ref[1:2, 0:HD],
                        wbh_ref[...], wbg_ref[...], wout_ref[...])
        y1_ref[...] = t["y1"]
        mrg_ref[...] = t["merged"]
        oghg_ref[...] = t["og_hg"]
        oggla_ref[...] = t["og_gla"]
        n1, _ = _rms(t["y1"])
        z2 = x_ref[...] + n1 * post1 * gate1
        n2, r2 = _rms(z2)
        nw2 = n2 * pre2
        h2 = (nw2 * (1.0 + scale2) + shift2).astype(BF16)
        h2_ref[...] = h2
        u = jnp.dot(h2, wg_ref[...], preferred_element_type=F32)
        v = jnp.dot(h2, wu_ref[...], preferred_element_type=F32)
        su = _silu(u)
        a = (su * v).astype(BF16)
        a_ref[...] = a
        y2 = jnp.dot(a, wd_ref[...], preferred_element_type=F32)
        n3, r3 = _rms(y2)
        z3 = z2 + n3 * post2 * gate2
        err = z3 - t_ref[...]
        part = 0.5 * inv_d * jnp.sum(err * err)
        dz3 = err * inv_d
        dgate2 = _colsum(dz3 * n3 * post2)
        tt = dz3 * gate2
        dpost2 = _colsum(tt * n3)
        dy2 = _rms_bwd(tt * post2, n3, r3).astype(BF16)
        dy2_ref[...] = dy2
        da = _dot_nt(dy2, wd_ref[...])
        du = (da * v * _dsilu(u)).astype(BF16)
        dv = (da * su).astype(BF16)
        du_ref[...] = du
        dv_ref[...] = dv
        dh2 = _dot_nt(du, wg_ref[...]) + _dot_nt(dv, wu_ref[...])
        dshift2 = _colsum(dh2)
        dscale2 = _colsum(dh2 * nw2)
        dnw2 = dh2 * (1.0 + scale2)
        dpre2 = _colsum(dnw2 * n2)
        dz2_ref[...] = dz3 + _rms_bwd(dnw2 * pre2, n2, r2)

        @pl.when(i == 0)
        def _():
            stat_ref[...] = jnp.zeros_like(stat_ref)
            loss_ref[...] = jnp.zeros_like(loss_ref)

        for r, val in enumerate((dshift2, dscale2, dgate2, dpre2, dpost2)):
            stat_ref[r:r + 1, :] += val
        loss_ref[...] += part

    tm = TM_FFN
    ctx_tiles = n_ctx_tiles * (TM // tm)
    lat = lambda i: (i, 0)
    full = lambda i: (i + ctx_tiles, 0)
    fixed = lambda i: (0, 0)

    def pcol(blk):
        return pl.BlockSpec((tm, HW), lambda i: (i + ctx_tiles, blk))

    in_specs = ([pl.BlockSpec((tm, d), lat)] + [pl.BlockSpec((tm, HW), full)] * 4
                + [pcol(C_HGATE), pcol(C_GGATE), pcol(9), pcol(10), pcol(11), pcol(12)]
                + [pl.BlockSpec((8, d), fixed), pl.BlockSpec((8, d), fixed), pl.BlockSpec((8, d), fixed)]
                + [VMEM_SPEC] * 6 + [pl.BlockSpec((tm, d), lat)])
    bf = lambda w: jax.ShapeDtypeStruct((rows, w), BF16)
    out_shape = [jax.ShapeDtypeStruct((8, 128), F32), jax.ShapeDtypeStruct((rows, d), F32),
                 jax.ShapeDtypeStruct((rows, d), F32), bf(d), bf(HW), bf(HW), bf(d), bf(dff), bf(dff), bf(dff), bf(d),
                 jax.ShapeDtypeStruct((8, d), F32)]
    out_specs = [pl.BlockSpec((8, 128), fixed), pl.BlockSpec((tm, d), lat), pl.BlockSpec((tm, d), lat),
                 pl.BlockSpec((tm, d), lat), pl.BlockSpec((tm, HW), lat), pl.BlockSpec((tm, HW), lat),
                 pl.BlockSpec((tm, d), lat), pl.BlockSpec((tm, dff), lat), pl.BlockSpec((tm, dff), lat),
                 pl.BlockSpec((tm, dff), lat), pl.BlockSpec((tm, d), lat), pl.BlockSpec((8, d), fixed)]
    return pl.pallas_call(
        body, name=name, grid=(rows // tm,), out_shape=out_shape, in_specs=in_specs, out_specs=out_specs,
        compiler_params=_cparams(dimension_semantics=("arbitrary",)),
    )(x_lat, *o_list, p, p, p, p, p, p, modx, norms, onorms, w_br_hg, w_br_gla, w_out, w_gate, w_up, w_down, target)


def _mixer_tail_bwd(x_lat, p, o_list, dz2, y1, modx, norms, onorms, w_br_hg, w_br_gla, w_out, n_ctx_tiles, n_tiles,
                    name):
    rows, d = x_lat.shape
    total = n_tiles * TM

    def body(x_ref, ofw_hg, obw_hg, ofw_gla, obw_gla, p_hgate, p_ggate, p_ghg_a, p_ghg_b, p_ggla_a, p_ggla_b,
             dz2_ref, y1_ref, modx_ref, norm_ref, on_ref, wbh_ref, wbg_ref, wout_ref,
             dohg_ref, dogla_ref, dhgate_ref, dggate_ref, dghg_ref, dggla_ref, dy1_ref, dbhg_ref, dbgla_ref,
             stat_ref):
        i = pl.program_id(0)

        @pl.when(i == 0)
        def _():
            stat_ref[...] = jnp.zeros_like(stat_ref)

        @pl.when(i < n_ctx_tiles)
        def _():
            for ref in (dohg_ref, dogla_ref, dhgate_ref, dggate_ref, dghg_ref, dggla_ref):
                ref[...] = jnp.zeros_like(ref)

        @pl.when(i >= n_ctx_tiles)
        def _():
            post1, gate1 = norm_ref[1:2, :], modx_ref[2:3, :]
            hg_on, gla_on = on_ref[0:1, 0:HD], on_ref[1:2, 0:HD]
            p_gate_hg = jnp.concatenate([p_ghg_a[...], p_ghg_b[...]], axis=1)
            p_gate_gla = jnp.concatenate([p_ggla_a[...], p_ggla_b[...]], axis=1)
            ph, pg = p_hgate[...], p_ggate[...]
            t = _mixer_tail(x_ref[...], ofw_hg[...] + obw_hg[...], ofw_gla[...] + obw_gla[...], ph, pg,
                            p_gate_hg, p_gate_gla, hg_on, gla_on, wbh_ref[...], wbg_ref[...], wout_ref[...])
            dz2 = dz2_ref[...]
            n1, r1 = _rms(y1_ref[...])
            dgate1 = _colsum(dz2 * n1 * post1)
            tt = dz2 * gate1
            dpost1 = _colsum(tt * n1)
            dy1 = _rms_bwd(tt * post1, n1, r1).astype(BF16)
            dy1_ref[...] = dy1
            dmerged = _dot_nt(dy1, wout_ref[...])
            dghg_ref[...] = dmerged * t["b_hg"] * t["s_hg"] * (1.0 - t["s_hg"])
            dggla_ref[...] = dmerged * t["b_gla"] * t["s_gla"] * (1.0 - t["s_gla"])
            db_hg = (dmerged * t["s_hg"]).astype(BF16)
            db_gla = (dmerged * t["s_gla"]).astype(BF16)
            dbhg_ref[...] = db_hg
            dbgla_ref[...] = db_gla
            don_acc = []
            for (db, wb, pgate, on, ns, rs, gain, gate_ref, do_ref) in (
                    (db_hg, wbh_ref, ph, t["on_hg"], t["n_hg"], t["r_hg"], hg_on, dhgate_ref, dohg_ref),
                    (db_gla, wbg_ref, pg, t["on_gla"], t["n_gla"], t["r_gla"], gla_on, dggate_ref, dogla_ref)):
                dog = _dot_nt(db, wb[...])
                gate_ref[...] = dog * on * _dsilu(pgate)
                don = dog * _silu(pgate)
                acc = jnp.zeros((1, HD), F32)
                for h in range(NH):
                    sl = slice(h * HD, (h + 1) * HD)
                    acc = acc + _colsum(don[:, sl] * ns[h])
                    do_ref[:, sl] = _rms_bwd(don[:, sl] * gain, ns[h], rs[h])
                don_acc.append(acc)
            stat_ref[0:1, :] += dgate1
            stat_ref[1:2, :] += dpost1
            stat_ref[2:3, 0:HD] += don_acc[0]
            stat_ref[2:3, HD:2 * HD] += don_acc[1]

    lat = lambda i: (jnp.maximum(i - n_ctx_tiles, 0), 0)
    full = lambda i: (i, 0)
    fixed = lambda i: (0, 0)

    def pcol(blk):
        return pl.BlockSpec((TM, HW), lambda i: (i, blk))

    in_specs = ([pl.BlockSpec((TM, d), lat)] + [pl.BlockSpec((TM, HW), full)] * 4
                + [pcol(C_HGATE), pcol(C_GGATE), pcol(9), pcol(10), pcol(11), pcol(12)]
                + [pl.BlockSpec((TM, d), lat), pl.BlockSpec((TM, d), lat)]
                + [pl.BlockSpec((8, d), fixed)] * 3 + [VMEM_SPEC] * 3)
    f = lambda w: jax.ShapeDtypeStruct((total, w), F32)
    out_shape = [f(HW), f(HW), f(HW), f(HW), f(d), f(d), jax.ShapeDtypeStruct((rows, d), BF16),
                 jax.ShapeDtypeStruct((rows, d), BF16), jax.ShapeDtypeStruct((rows, d), BF16),
                 jax.ShapeDtypeStruct((8, d), F32)]
    out_specs = ([pl.BlockSpec((TM, HW), full)] * 4 + [pl.BlockSpec((TM, d), full)] * 2
                 + [pl.BlockSpec((TM, d), lat)] * 3 + [pl.BlockSpec((8, d), fixed)])
    return pl.pallas_call(
        body, name=name, grid=(n_tiles,), out_shape=out_shape, in_specs=in_specs, out_specs=out_specs,
        compiler_params=_cparams(dimension_semantics=("arbitrary",)),
    )(x_lat, *o_list, p, p, p, p, p, p, dz2, y1, modx, norms, onorms, w_br_hg, w_br_gla, w_out)


def _in_projection_bwd(z, dz2, modc, modx, pre1, w_r, pieces, n_ctx_tiles, name):
    rows, d = z.shape
    lat_rows = dz2.shape[0]
    width = w_r.shape[1]
    n_pieces = len(pieces)

    def body(*refs):
        z_ref, dz2_ref, modc_ref, modx_ref, pre_ref, w_ref = refs[:6]
        (dhq_f, dhq_b, dhi_f, dhi_b, dhf_f, dhf_b, dhgate, dgq_f, dgq_b, dgk_f, dgk_b, dgv_f, dgv_b, dggate,
         dghg, dggla, dlr_f, dlr_b) = refs[6:6 + n_pieces]
        dp_ref, gx_ref, stat_ref = refs[6 + n_pieces:]
        i = pl.program_id(0)
        is_ctx = i < n_ctx_tiles
        sections = [
            (0, dhq_f[...] + dhq_b[...]), (HW, dhi_f[...] + dhi_b[...]), (2 * HW, dhf_f[...]), (3 * HW, dhf_b[...]),
            (4 * HW, dhgate[...]), (5 * HW, dgq_f[...] + dgq_b[...]), (6 * HW, dgk_f[...] + dgk_b[...]),
            (7 * HW, dgv_f[...] + dgv_b[...]), (8 * HW, dggate[...]),
            (9 * HW, dghg[:, 0:HW]), (10 * HW, dghg[:, HW:2 * HW]),
            (11 * HW, dggla[:, 0:HW]), (12 * HW, dggla[:, HW:2 * HW]), (OFF_LR, dlr_f[...] + dlr_b[...])]
        dh = jnp.zeros((TM, d), F32)
        for off, val in sections:
            w = val.shape[1]
            vb = val.astype(BF16)
            dp_ref[:, off:off + w] = vb
            dh = dh + _dot_nt(vb, w_ref[:, off:off + w])
        n, r = _rms(z_ref[...])
        pre = pre_ref[...]
        scale = jnp.where(is_ctx, modc_ref[1:2, :], modx_ref[1:2, :])
        nw = n * pre
        dshift = _colsum(dh)
        dscale = _colsum(dh * nw)
        dnw = dh * (1.0 + scale)
        dpre = _colsum(dnw * n)
        gx_ref[...] = dz2_ref[...] + _rms_bwd(dnw * pre, n, r)
        zero = jnp.zeros((1, d), F32)

        @pl.when(i == 0)
        def _():
            stat_ref[...] = jnp.zeros_like(stat_ref)

        stat_ref[0:1, :] += jnp.where(is_ctx, zero, dshift)
        stat_ref[1:2, :] += jnp.where(is_ctx, zero, dscale)
        stat_ref[2:3, :] += jnp.where(is_ctx, dshift, zero)
        stat_ref[3:4, :] += jnp.where(is_ctx, dscale, zero)
        stat_ref[4:5, :] += dpre

    full = lambda i: (i, 0)
    lat = lambda i: (jnp.maximum(i - n_ctx_tiles, 0), 0)
    fixed = lambda i: (0, 0)
    piece_specs = [pl.BlockSpec((TM, a.shape[1]), full) for a in pieces]
    in_specs = [pl.BlockSpec((TM, d), full), pl.BlockSpec((TM, d), lat), pl.BlockSpec((8, d), fixed),
                pl.BlockSpec((8, d), fixed), pl.BlockSpec((1, d), fixed), VMEM_SPEC] + piece_specs
    return pl.pallas_call(
        body, name=name, grid=(rows // TM,),
        out_shape=[jax.ShapeDtypeStruct((rows, width), BF16), jax.ShapeDtypeStruct((lat_rows, d), F32),
                   jax.ShapeDtypeStruct((8, d), F32)],
        in_specs=in_specs,
        out_specs=[pl.BlockSpec((TM, width), full), pl.BlockSpec((TM, d), lat), pl.BlockSpec((8, d), fixed)],
        compiler_params=_cparams(dimension_semantics=("arbitrary",)),
    )(z, dz2, modc, modx, pre1, w_r, *pieces)


def _weight_grad(xs, dy, name, col_block=None, tk=256, tn=512):
    rows = dy.shape[0]
    k = xs.shape[1]
    if col_block is None:
        n, tn_, cb = dy.shape[1], tn, 0
    else:
        n, tn_, cb = col_block[0], col_block[0], col_block[1]
    tn_ = min(tn_, n)
    tk_ = min(tk, k)

    def body(x_ref, dy_ref, o_ref):
        o_ref[...] = _dot_tn(x_ref[...], dy_ref[...])

    return pl.pallas_call(
        body, name=name, grid=(k // tk_, n // tn_),
        out_shape=jax.ShapeDtypeStruct((k, n), F32),
        in_specs=[pl.BlockSpec((rows, tk_), lambda i, j: (0, i)),
                  pl.BlockSpec((rows, tn_), lambda i, j: (0, j + cb))],
        out_specs=pl.BlockSpec((tk_, tn_), lambda i, j: (i, j)),
        compiler_params=_cparams(dimension_semantics=("parallel", "parallel")),
    )(xs, dy)


def _chunk_terms(q, k, g, fw):
    c = CHUNK
    r = lax.broadcasted_iota(jnp.int32, (c, c), 0)
    s = lax.broadcasted_iota(jnp.int32, (c, c), 1)
    causal = (s <= r) if fw else (s >= r)
    causal_t = (s >= r) if fw else (s <= r)
    cum = jnp.dot(causal.astype(F32), g, precision=HIGHEST, preferred_element_type=F32)
    row = lax.broadcasted_iota(jnp.int32, (c, 1), 0)
    pos = row if fw else (c - 1 - row)
    starts = [None]
    for j in range(1, NSUB):
        rj = SUB * j - 1 if fw else c - SUB * j
        starts.append(cum[rj:rj + 1, :])
    in_blk = [(pos >= SUB * j) & (pos < SUB * (j + 1)) for j in range(NSUB)]
    e = [jnp.exp(cum)]
    for j in range(1, NSUB):
        e.append(jnp.exp(jnp.where(pos >= SUB * j, cum - starts[j], -1e30)))
    own = jnp.zeros_like(cum)
    for j in range(1, NSUB):
        own = own + jnp.where(in_blk[j], starts[j], 0.0)
    kscale = jnp.exp(own - cum)
    rend = c - 1 if fw else 0
    cend = cum[rend:rend + 1, :]
    tail = jnp.exp(cend - cum)
    qcat = jnp.concatenate([q * e[j] for j in range(NSUB)], axis=1).astype(BF16)
    kt = k * kscale
    km = jnp.concatenate([jnp.where(in_blk[j], kt, 0.0) for j in range(NSUB)], axis=1).astype(BF16)
    return dict(causal=causal, causal_t=causal_t, e=e, in_blk=in_blk, kscale=kscale, cend=cend, tail=tail,
                qcat=qcat, km=km)


def _chunk_fwd(q, k, v, g, st0, fw):
    t = _chunk_terms(q, k, g, fw)
    a = jnp.where(t["causal"], _dot_nt(t["qcat"], t["km"]), 0.0)
    o = _dot(a, v) + _dot_nt(t["qcat"][:, 0:HD], st0)
    st1 = st0 * jnp.exp(t["cend"]) + _dot_tn(v, k * t["tail"])
    return o, st1


def _chunk_bwd(q, k, v, g, st0, do, dst1, fw):
    t = _chunk_terms(q, k, g, fw)
    qcat, km, e = t["qcat"], t["km"], t["e"]
    a_t = jnp.where(t["causal_t"], _dot_nt(km, qcat), 0.0)
    ktail = k * t["tail"]
    dv = _dot(a_t, do) + _dot_nt(ktail, dst1)
    da = jnp.where(t["causal"], _dot_nt(do, v), 0.0)
    da_t = jnp.where(t["causal_t"], _dot_nt(v, do), 0.0)
    dqcat = _dot(da, km)
    dq_inter = e[0] * _dot(do, st0)
    dq = dq_inter
    for j in range(NSUB):
        dq = dq + e[j] * dqcat[:, j * HD:(j + 1) * HD]
    dkm = _dot(da_t, qcat)
    dkt = jnp.zeros_like(k)
    for j in range(NSUB):
        dkt = dkt + jnp.where(t["in_blk"][j], dkm[:, j * HD:(j + 1) * HD], 0.0)
    dk_inter = _dot(v, dst1) * t["tail"]
    dk = dkt * t["kscale"] + dk_inter
    dcum = q * dq_inter - k * dk_inter
    for j in range(NSUB):
        sl = slice(j * HD, (j + 1) * HD)
        dcum = dcum + qcat[:, sl].astype(F32) * dqcat[:, sl] - km[:, sl].astype(F32) * dkm[:, sl]
    ecend = jnp.exp(t["cend"])
    end = ecend * _colsum(st0 * dst1) + _colsum(k * dk_inter)
    dg = jnp.dot(t["causal_t"].astype(F32), dcum, precision=HIGHEST, preferred_element_type=F32) + end
    dst0 = dst1 * ecend + _dot_tn(do, q * e[0])
    return dq, dk, dv, dg, dst0


def _chunk_index(step, n_ctx_chunks, n_chunks, fw):
    if fw:
        return step
    return jnp.where(step < n_ctx_chunks, n_ctx_chunks - 1 - step, n_chunks - 1 + n_ctx_chunks - step)


def _hg_inputs(hq, hf, lbv, d_idx, sl):
    lb = _sigmoid(lbv[d_idx:d_idx + 1, sl] - lbv[2 + d_idx:3 + d_idx, sl])
    sg = _sigmoid(hf)
    f = lb + (1.0 - lb) * sg
    return _silu(hq), 1.0 - f, jnp.log(f), f, sg, lb


def _scan_fwd(p, side, n_ctx_chunks, fw, branch, name):
    rows = p.shape[0]
    n_chunks = rows // CHUNK
    d_idx = 0 if fw else 1
    hg = branch == "hg"
    cols = (C_HQ, C_HI, C_HF_FW + d_idx) if hg else (C_GQ, C_GK, C_GV)

    def body(*refs):
        if hg:
            a_ref, b_ref, c_ref, lb_ref, o_ref, st_ref, state = refs
        else:
            a_ref, b_ref, c_ref, lr_ref, wgk_ref, bgk_ref, o_ref, st_ref, state = refs
            logits = _dot(lr_ref[...], wgk_ref[...]) + bgk_ref[...]
            g_all = _log_sigmoid(logits) * (1.0 / GATE_NORM)

        @pl.when(pl.program_id(0) == 0)
        def _():
            state[...] = jnp.zeros_like(state)

        for h in range(NH):
            sl = slice(h * HD, (h + 1) * HD)
            if hg:
                q, k, g, _, _, _ = _hg_inputs(a_ref[:, sl], c_ref[:, sl], lb_ref[...], d_idx, sl)
                v = b_ref[:, sl]
            else:
                q, k, v, g = a_ref[:, sl] * (HD ** -0.5), b_ref[:, sl], c_ref[:, sl], g_all[:, sl]
            st0 = state[h]
            st_ref[0, h] = st0
            o, st1 = _chunk_fwd(q, k, v, g, st0, fw)
            o_ref[:, sl] = o
            state[h] = st1

    def cmap(blk):
        return pl.BlockSpec((CHUNK, HW), lambda j: (_chunk_index(j, n_ctx_chunks, n_chunks, fw), blk))

    fixed = lambda j: (0, 0)
    in_specs = [cmap(cols[0]), cmap(cols[1]), cmap(cols[2])]
    if hg:
        in_specs += [pl.BlockSpec((4, HW), fixed)]
        args = (p, p, p, side)
    else:
        in_specs += [pl.BlockSpec((CHUNK, 128), lambda j: (_chunk_index(j, n_ctx_chunks, n_chunks, fw), OFF_LR // 128)),
                     pl.BlockSpec((128, HW), fixed), pl.BlockSpec((1, HW), fixed)]
        args = (p, p, p, p, side[0], side[1])
    return pl.pallas_call(
        body, name=name, grid=(n_chunks,),
        out_shape=[jax.ShapeDtypeStruct((rows, HW), F32), jax.ShapeDtypeStruct((n_chunks, NH, HD, HD), F32)],
        in_specs=in_specs,
        out_specs=[pl.BlockSpec((CHUNK, HW), lambda j: (_chunk_index(j, n_ctx_chunks, n_chunks, fw), 0)),
                   pl.BlockSpec((1, NH, HD, HD), lambda j: (_chunk_index(j, n_ctx_chunks, n_chunks, fw), 0, 0, 0))],
        scratch_shapes=[pltpu.VMEM((NH, HD, HD), F32)],
        compiler_params=_cparams(dimension_semantics=("arbitrary",)),
    )(*args)


def _scan_bwd(p, side, states, d_o, n_ctx_chunks, fw, branch, name):
    rows = p.shape[0]
    n_chunks = rows // CHUNK
    d_idx = 0 if fw else 1
    hg = branch == "hg"
    cols = (C_HQ, C_HI, C_HF_FW + d_idx) if hg else (C_GQ, C_GK, C_GV)

    def body(*refs):
        if hg:
            a_ref, b_ref, c_ref, lb_ref, st_ref, do_ref, da_ref, db_ref, dc_ref, dlb_ref, dstate = refs
        else:
            (a_ref, b_ref, c_ref, lr_ref, wgk_ref, bgk_ref, st_ref, do_ref, da_ref, db_ref, dc_ref, dlr_ref,
             dwgk_ref, dbias_ref, dstate) = refs
            lr = lr_ref[...]
            logits = _dot(lr, wgk_ref[...]) + bgk_ref[...]
            g_all = _log_sigmoid(logits) * (1.0 / GATE_NORM)

        @pl.when(pl.program_id(0) == 0)
        def _():
            dstate[...] = jnp.zeros_like(dstate)
            if hg:
                dlb_ref[...] = jnp.zeros_like(dlb_ref)
            else:
                dwgk_ref[...] = jnp.zeros_like(dwgk_ref)
                dbias_ref[...] = jnp.zeros_like(dbias_ref)

        dg_parts = []
        for h in range(NH):
            sl = slice(h * HD, (h + 1) * HD)
            if hg:
                hq, hf = a_ref[:, sl], c_ref[:, sl]
                q, k, g, f, sg, lb = _hg_inputs(hq, hf, lb_ref[...], d_idx, sl)
                v = b_ref[:, sl]
            else:
                q, k, v, g = a_ref[:, sl] * (HD ** -0.5), b_ref[:, sl], c_ref[:, sl], g_all[:, sl]
            dq, dk, dv, dg, dst0 = _chunk_bwd(q, k, v, g, st_ref[0, h], do_ref[:, sl], dstate[h], fw)
            dstate[h] = dst0
            if hg:
                da_ref[:, sl] = dq * _dsilu(hq)
                db_ref[:, sl] = dv
                df = dg / f - dk
                dc_ref[:, sl] = df * (1.0 - lb) * sg * (1.0 - sg)
                dlb_ref[0:1, sl] += _colsum(df * (1.0 - sg))
            else:
                da_ref[:, sl] = dq * (HD ** -0.5)
                db_ref[:, sl] = dk
                dc_ref[:, sl] = dv
                dg_parts.append(dg)
        if not hg:
            dlogits = jnp.concatenate(dg_parts, axis=1) * (1.0 / GATE_NORM) * (1.0 - _sigmoid(logits))
            dlr_ref[...] = _dot_nt(dlogits, wgk_ref[...])
            dwgk_ref[...] += _dot_tn(lr, dlogits)
            dbias_ref[0:1, :] += _colsum(dlogits)

    def chunk_of(j):
        return _chunk_index(n_chunks - 1 - j, n_ctx_chunks, n_chunks, fw)

    def cmap(blk, width=HW):
        return pl.BlockSpec((CHUNK, width), lambda j: (chunk_of(j), blk))

    fixed = lambda j: (0, 0)
    st_spec = pl.BlockSpec((1, NH, HD, HD), lambda j: (chunk_of(j), 0, 0, 0))
    big = jax.ShapeDtypeStruct((rows, HW), F32)
    if hg:
        in_specs = [cmap(cols[0]), cmap(cols[1]), cmap(cols[2]), pl.BlockSpec((4, HW), fixed), st_spec, cmap(0)]
        args = (p, p, p, side, states, d_o)
        out_shape = [big, big, big, jax.ShapeDtypeStruct((8, HW), F32)]
        out_specs = [cmap(0), cmap(0), cmap(0), pl.BlockSpec((8, HW), fixed)]
    else:
        in_specs = [cmap(cols[0]), cmap(cols[1]), cmap(cols[2]), cmap(OFF_LR // 128, 128),
                    pl.BlockSpec((128, HW), fixed), pl.BlockSpec((1, HW), fixed), st_spec, cmap(0)]
        args = (p, p, p, p, side[0], side[1], states, d_o)
        out_shape = [big, big, big, jax.ShapeDtypeStruct((rows, 128), F32), jax.ShapeDtypeStruct((128, HW), F32),
                     jax.ShapeDtypeStruct((8, HW), F32)]
        out_specs = [cmap(0), cmap(0), cmap(0), cmap(0, 128), pl.BlockSpec((128, HW), fixed),
                     pl.BlockSpec((8, HW), fixed)]
    return pl.pallas_call(
        body, name=name, grid=(n_chunks,), out_shape=out_shape, in_specs=in_specs, out_specs=out_specs,
        scratch_shapes=[pltpu.VMEM((NH, HD, HD), F32)],
        compiler_params=_cparams(dimension_semantics=("arbitrary",)),
    )(*args)


SMALL_ROWS = 56
ROWS_MOD_X = (0, 1, 8, 16, 17, 18)
ROWS_MOD_C = (2, 3)
ROW_PRE1, ROW_POST1, ROW_ONORM, ROW_PRE2, ROW_POST2, ROW_LB, ROW_BGK, ROW_WGK = 4, 9, 10, 19, 20, 24, 32, 40


def _reduce_small(gathered, lb_full, name):
    _, _, d = gathered.shape

    def body(g_ref, lb_ref, sum_ref, dmod_ref, dbmod_ref, dlb_ref):
        total = g_ref[0]
        for b in range(1, N_DEV):
            total = total + g_ref[b]
        sum_ref[...] = total
        dmod_ref[...] = jnp.zeros_like(dmod_ref)
        for m in range(N_MOD):
            col = slice(m * d, (m + 1) * d)
            acc = jnp.zeros((1, d), F32)
            for b in range(N_DEV):
                row = g_ref[b, ROWS_MOD_X[m]:ROWS_MOD_X[m] + 1, :]
                dmod_ref[b:b + 1, col] = row
                acc = acc + row
            if m < 2:
                ctx_row = total[ROWS_MOD_C[m]:ROWS_MOD_C[m] + 1, :]
                dmod_ref[8:9, col] = ctx_row
                acc = acc + ctx_row
            dbmod_ref[:, col] = acc
        lbv = lb_ref[...]
        for dd in range(2):
            lb = _sigmoid(lbv[dd:dd + 1, :] - lbv[2 + dd:3 + dd, :])
            gl = total[ROW_LB:ROW_LB + 1, dd * HW:(dd + 1) * HW] * lb * (1.0 - lb)
            dlb_ref[dd:dd + 1, :] = gl
            dlb_ref[2 + dd:3 + dd, :] = -gl

    return pl.pallas_call(
        body, name=name,
        out_shape=[jax.ShapeDtypeStruct((SMALL_ROWS, d), F32), jax.ShapeDtypeStruct((16, N_MOD * d), F32),
                   jax.ShapeDtypeStruct((1, N_MOD * d), F32), jax.ShapeDtypeStruct((4, HW), F32)],
        in_specs=[VMEM_SPEC] * 2, out_specs=[VMEM_SPEC] * 4, compiler_params=_cparams(),
    )(gathered, lb_full)


def _c_ctx_grad(gathered, c_ctx_row, name):
    def body(g_ref, c_ref, o_ref):
        acc = g_ref[0, 0:1, :]
        for chip in range(1, N_CHIP):
            acc = acc + g_ref[2 * chip, 0:1, :]
        o_ref[...] = acc * _dsilu(c_ref[...])

    return pl.pallas_call(
        body, name=name, out_shape=jax.ShapeDtypeStruct(c_ctx_row.shape, F32),
        in_specs=[VMEM_SPEC] * 2, out_specs=VMEM_SPEC, compiler_params=_cparams(),
    )(gathered, c_ctx_row)


def _relayout_w_in(w):
    pad = jnp.zeros((w.shape[0], 128 - 2 * RANK), w.dtype)
    return jnp.concatenate([w[:, :9 * HW], w[:, 9 * HW + 2 * RANK:], w[:, 9 * HW:9 * HW + 2 * RANK], pad], axis=1)


def _w_in_grad_blocks(g_main, g_lr, n_blocks):
    lr0 = 9 * HW
    n = (g_main.shape[1] + 2 * RANK) // n_blocks

    def cols(lo, hi):
        out = []
        if lo < lr0:
            out.append(g_main[:, lo:min(hi, lr0)])
        if hi > lr0 and lo < lr0 + 2 * RANK:
            out.append(g_lr[:, max(lo, lr0) - lr0:min(hi, lr0 + 2 * RANK) - lr0])
        if hi > lr0 + 2 * RANK:
            out.append(g_main[:, max(lo, lr0 + 2 * RANK) - 2 * RANK:hi - 2 * RANK])
        return out

    return jnp.stack([jnp.concatenate(cols(j * n, (j + 1) * n), axis=1) for j in range(n_blocks)])


def _blocked(full, n_blocks):
    k, n = full.shape
    return full.reshape(k, n_blocks, n // n_blocks).transpose(1, 0, 2)


def _unblocked(blocks):
    nb, k, n = blocks.shape
    return blocks.transpose(1, 0, 2).reshape(k, nb * n)


def _sample_front(x0, ctx0, modc, modx, norm_pre1, lb_full, gla_side, w_in_r):
    ctx_len = ctx0.shape[0]
    n_ctx_tiles = ctx_len // TM
    n_ctx_chunks = ctx_len // CHUNK
    z = jnp.concatenate([ctx0, x0], axis=0)
    h1, p = _in_projection(z, modc, modx, norm_pre1, w_in_r, n_ctx_tiles, "in_projection")
    o_hg_fw, st_hg_fw = _scan_fwd(p, lb_full, n_ctx_chunks, True, "hg", "scan_hg_fw")
    o_hg_bw, st_hg_bw = _scan_fwd(p, lb_full, n_ctx_chunks, False, "hg", "scan_hg_bw")
    o_gla_fw, st_gla_fw = _scan_fwd(p, gla_side[0], n_ctx_chunks, True, "gla", "scan_gla_fw")
    o_gla_bw, st_gla_bw = _scan_fwd(p, gla_side[1], n_ctx_chunks, False, "gla", "scan_gla_bw")
    return dict(z=z, h1=h1, p=p, o_list=[o_hg_fw, o_hg_bw, o_gla_fw, o_gla_bw],
                states=[st_hg_fw, st_hg_bw, st_gla_fw, st_gla_bw])


def _sample_back(reduce, front, x0, ctx0, target0, modc, modx, norm_pre1, norms, onorms, lb_full, gla_side, w_in_r,
                 wbh, wbg, wout, wg, wu, wd):
    seq, d = x0.shape
    ctx_len = ctx0.shape[0]
    n_ctx_tiles = ctx_len // TM
    n_tiles = (ctx_len + seq) // TM
    n_ctx_chunks = ctx_len // CHUNK
    z, h1, p, o_list = front["z"], front["h1"], front["p"], front["o_list"]
    st_hg_fw, st_hg_bw, st_gla_fw, st_gla_bw = front["states"]
    (loss_part, dz2, y1, merged, og_hg, og_gla, h2, a_act, du, dv, dy2, stat_ffn) = _mixer_ffn(
        x0, p, o_list, modx, norms, onorms, wbh, wbg, wout, wg, wu, wd, target0, n_ctx_tiles, "mixer_ffn")
    dff = wg.shape[1]
    tn_ff = dff // N_CHIP if (dff // N_CHIP) % 128 == 0 else 256
    tok = reduce("ffn", [_weight_grad(h2, du, "grad_w_ff_gate", tn=tn_ff),
                         _weight_grad(h2, dv, "grad_w_ff_up", tn=tn_ff),
                         _weight_grad(a_act, dy2, "grad_w_ff_down", tk=tn_ff)])

    (d_ohg, d_ogla, d_hgate, d_ggate, d_ghg, d_ggla, dy1, db_hg, db_gla, stat_mix) = _mixer_tail_bwd(
        x0, p, o_list, dz2, y1, modx + tok, norms, onorms, wbh, wbg, wout, n_ctx_tiles, n_tiles, "mixer_tail_bwd")
    tok = reduce("mix", [_weight_grad(og_hg, db_hg, "grad_w_br_hg"), _weight_grad(og_gla, db_gla, "grad_w_br_gla"),
                         _weight_grad(merged, dy1, "grad_w_out")])
    lb_b = lb_full + tok
    gla_b = [(wgk, bias + tok) for wgk, bias in gla_side]
    dhq_f, dhi_f, dhf_f, dlb_f = _scan_bwd(p, lb_b, st_hg_fw, d_ohg, n_ctx_chunks, True, "hg", "scan_hg_fw_bwd")
    dhq_b, dhi_b, dhf_b, dlb_b = _scan_bwd(p, lb_b, st_hg_bw, d_ohg, n_ctx_chunks, False, "hg", "scan_hg_bw_bwd")
    dgq_f, dgk_f, dgv_f, dlr_f, dwgk_f, dbgk_f = _scan_bwd(p, gla_b[0], st_gla_fw, d_ogla, n_ctx_chunks, True, "gla",
                                                           "scan_gla_fw_bwd")
    dgq_b, dgk_b, dgv_b, dlr_b, dwgk_b, dbgk_b = _scan_bwd(p, gla_b[1], st_gla_bw, d_ogla, n_ctx_chunks, False, "gla",
                                                           "scan_gla_bw_bwd")
    pieces = [dhq_f, dhq_b, dhi_f, dhi_b, dhf_f, dhf_b, d_hgate, dgq_f, dgq_b, dgk_f, dgk_b, dgv_f, dgv_b, d_ggate,
              d_ghg, d_ggla, dlr_f, dlr_b]
    dp, grad_x, stat_in = _in_projection_bwd(z, dz2, modc, modx, norm_pre1, w_in_r, pieces, n_ctx_tiles,
                                             "in_projection_bwd")

    reduce("small", dict(stat_in=stat_in, stat_mix=stat_mix, stat_ffn=stat_ffn, dlb=(dlb_f, dlb_b),
                         dwgk=(dwgk_f, dwgk_b), dbgk=(dbgk_f, dbgk_b)))
    g_in_main = _weight_grad_cols(h1, dp, OFF_LR, "grad_w_in_main")
    g_in_lr = _weight_grad(h1, dp, "grad_w_in_lr", col_block=(128, OFF_LR // 128))
    reduce("in", [_w_in_grad_blocks(g_in_main, g_in_lr, N_CHIP)])
    return dict(
        loss_part=loss_part, grad_x=grad_x, stat_in=stat_in, stat_mix=stat_mix, stat_ffn=stat_ffn,
        dlb=(dlb_f, dlb_b), dwgk=(dwgk_f, dwgk_b), dbgk=(dbgk_f, dbgk_b))


def kernel(x, c, ctx, c_ctx, w_mod, b_mod, norm_pre1, norm_post1, norm_pre2, norm_post2, w_in, hg_lb, hg_onorm, gla_w_gk, gla_b_gk, gla_onorm, w_br_hg, w_br_gla, w_out, w_ff_gate, w_ff_up, w_ff_down, loss_target, m_c_ctx, m_w_mod, m_b_mod, m_norm_pre1, m_norm_post1, m_norm_pre2, m_norm_post2, m_w_in, m_hg_lb, m_hg_onorm, m_gla_w_gk, m_gla_b_gk, m_gla_onorm, m_w_br_hg, m_w_br_gla, m_w_out, m_w_ff_gate, m_w_ff_up, m_w_ff_down, v_c_ctx, v_w_mod, v_b_mod, v_norm_pre1, v_norm_post1, v_norm_pre2, v_norm_post2, v_w_in, v_hg_lb, v_hg_onorm, v_gla_w_gk, v_gla_b_gk, v_gla_onorm, v_w_br_hg, v_w_br_gla, v_w_out, v_w_ff_gate, v_w_ff_up, v_w_ff_down):
    seq, d = x.shape[1], x.shape[2]
    ctx_len = ctx.shape[1]
    assert seq % TM == 0 and ctx_len % TM == 0 and d == 2 * HW
    ax, ay, ac = lax.axis_index("x"), lax.axis_index("y"), lax.axis_index("c")
    chip = 2 * ax + ay
    dev = 2 * chip + ac
    c_arr = jnp.reshape(ac, (1,)).astype(jnp.int32)

    nc = d // 128
    pad8 = lambda a: jnp.pad(a, ((0, -a.shape[0] % 8), (0, 0)))
    small1 = jnp.concatenate([c.reshape(nc, 128), pad8(hg_lb.reshape(4, 128)), gla_w_gk.reshape(2 * RANK, 128),
                              pad8(gla_b_gk.reshape(2, 128))], axis=0)
    got1 = _allgather8(small1, "gather_small_params")
    c_all = got1[:, :nc, :].reshape(N_DEV, d)
    per_chip = got1[0::2]
    lb_full = per_chip[:, nc:nc + 4, :].transpose(1, 0, 2).reshape(4, HW)
    wgk_full = per_chip[:, nc + 8:nc + 8 + 2 * RANK, :].transpose(1, 0, 2).reshape(2, RANK, HW)
    bgk_full = per_chip[:, nc + 8 + 2 * RANK:nc + 10 + 2 * RANK, :].transpose(1, 0, 2).reshape(2, HW)
    wgk_pad = [jnp.zeros((128, HW), F32).at[dd * RANK:(dd + 1) * RANK].set(wgk_full[dd]) for dd in range(2)]
    bgk = [bgk_full[dd:dd + 1] for dd in range(2)]

    n_mod_cols = w_mod.shape[2]
    cond = jnp.concatenate([c_all, pad8(c_ctx.reshape(1, d))], axis=0)
    b_cols = lax.dynamic_slice(b_mod, (0, chip * n_mod_cols), (1, n_mod_cols))
    mod_part = _mod_forward(cond, w_mod[0], b_cols, "mod_forward")
    mod_got = _allgather8(mod_part, "gather_mod")
    mod_all = mod_got[0::2].transpose(1, 0, 2).reshape(16, N_CHIP * n_mod_cols)
    modx = pad8(lax.dynamic_slice(mod_all, (dev, 0), (1, N_MOD * d)).reshape(N_MOD, d))
    modc = pad8(mod_all[8].reshape(N_MOD, d))

    chip_arr = jnp.reshape(chip, (1,)).astype(jnp.int32)
    blocks = [_cast_into_blocks(chip_arr, w_[0], "cast_" + nm) for w_, nm in (
        (w_in, "w_in"), (w_br_hg, "w_br_hg"), (w_br_gla, "w_br_gla"), (w_out, "w_out"), (w_ff_gate, "w_ff_gate"),
        (w_ff_up, "w_ff_up"), (w_ff_down, "w_ff_down"))]
    gathered_in = _gather_blocks(blocks[:1], "gather_w_in")
    sems, lands, token = _blocks_start(blocks[1:], "gather_rest_start")
    w_in_r = _relayout_w_in(_unblocked(gathered_in[0]))

    norms = jnp.concatenate([norm_pre1, norm_post1, norm_pre2, norm_post2, jnp.zeros((4, d), F32)], axis=0)
    onorms = jnp.zeros((8, d), F32).at[0, :HD].set(hg_onorm[0]).at[1, :HD].set(gla_onorm[0])
    gla_side = [(wgk_pad[dd], bgk[dd]) for dd in range(2)]
    modx = modx + token[0, 0]
    front = _sample_front(x[0], ctx[0], modc, modx, norm_pre1, lb_full, gla_side, w_in_r)
    lands = _blocks_wait(sems, lands, front["o_list"][3], "gather_rest_wait")
    gathered = _blocks_finish(lands, "gather_rest_finish")
    wbh, wbg = _unblocked(gathered[0]), _unblocked(gathered[1])
    wout = gathered[2].reshape(d, d)
    wg, wu = _unblocked(gathered[3]), _unblocked(gathered[4])
    wd = gathered[5].reshape(wg.shape[1], d)
    dff = wg.shape[1]
    groups = {"ffn": ["w_ff_gate", "w_ff_up", "w_ff_down"], "mix": ["w_br_hg", "w_br_gla", "w_out"], "in": ["w_in"]}
    row_sharded = {"w_out": d // N_CHIP, "w_ff_down": dff // N_CHIP}
    in_flight = {}

    small = {}

    def reduce_small(stats):
        small2 = jnp.concatenate([
            stats["stat_in"], stats["stat_mix"], stats["stat_ffn"],
            jnp.concatenate(stats["dlb"], axis=1), jnp.concatenate(stats["dbgk"], axis=1),
            jnp.concatenate([stats["dwgk"][0][0:RANK], stats["dwgk"][1][RANK:2 * RANK]], axis=1)], axis=0)
        assert small2.shape[0] == SMALL_ROWS
        got2 = _allgather8(small2, "gather_small_grads")
        total, dmod_all, g_b_mod, g_lb_full = _reduce_small(got2, lb_full, "reduce_small")
        dmod_cols = lax.dynamic_slice(dmod_all, (0, chip * n_mod_cols), (16, n_mod_cols))
        g_w_mod, cctx_part = _mod_backward(cond, w_mod[0], dmod_cols, "mod_backward")
        got3 = _allgather8(cctx_part, "gather_c_ctx_grad")
        g_c_ctx = _c_ctx_grad(got3, c_ctx.reshape(1, d), "c_ctx_grad")
        small.update(total=total, g_b_mod=g_b_mod, g_lb_full=g_lb_full, g_w_mod=g_w_mod, g_c_ctx=g_c_ctx)

    def reduce(group, grads):
        if group == "small":
            return reduce_small(grads)
        nms = groups[group]
        full = [g if g.ndim == 3 else g.reshape(N_CHIP, row_sharded[nm], d) if nm in row_sharded
                else _blocked(g, N_CHIP) for g, nm in zip(grads, nms)]
        from_sibling = _send_other_half(full, "grads_to_sibling_" + group)
        pairs = [_pair_sum(c_arr, f, r_, "pair_sum_" + nm) for f, r_, nm in zip(full, from_sibling, nms)]
        after = [small["g_c_ctx"], small["total"]] if group == "in" else []
        sems_, pairs, lands_, token_ = _scatter_start(pairs, "grads_to_owner_start_" + group, after)
        in_flight[group] = (sems_, pairs, lands_, token_)
        return token_[0, 0]

    r = _sample_back(reduce, front, x[0], ctx[0], loss_target[0], modc, modx, norm_pre1, norms, onorms, lb_full,
                     gla_side, w_in_r, wbh, wbg, wout, wg, wu, wd)
    loss_part, grad_x, stat_in, stat_mix, stat_ffn = (r[k] for k in ("loss_part", "grad_x", "stat_in", "stat_mix",
                                                                     "stat_ffn"))
    (dlb_f, dlb_b), (dwgk_f, dwgk_b), (dbgk_f, dbgk_b) = r["dlb"], r["dwgk"], r["dbgk"]

    weights = dict(w_in=(w_in, m_w_in, v_w_in), w_br_hg=(w_br_hg, m_w_br_hg, v_w_br_hg),
                   w_br_gla=(w_br_gla, m_w_br_gla, v_w_br_gla), w_out=(w_out, m_w_out, v_w_out),
                   w_ff_gate=(w_ff_gate, m_w_ff_gate, v_w_ff_gate), w_ff_up=(w_ff_up, m_w_ff_up, v_w_ff_up),
                   w_ff_down=(w_ff_down, m_w_ff_down, v_w_ff_down))
    names = ["w_in", "w_br_hg", "w_br_gla", "w_out", "w_ff_gate", "w_ff_up", "w_ff_down"]
    big = {}

    def finish(group, after):
        sems_, pairs, lands_, _ = in_flight[group]
        pairs, lands_ = _scatter_wait(sems_, pairs, lands_, after, "grads_to_owner_wait_" + group)
        own_half = [_sum_owner(chip_arr, pr, g, "chip_sum_" + nm) for pr, g, nm in zip(pairs, lands_, groups[group])]
        other_half = _swap_with_sibling(own_half, "halves_to_sibling_" + group)
        for nm, own, oth in zip(groups[group], own_half, other_half):
            w_, m_, v_ = weights[nm]
            res = _adamw_halves(c_arr, own, oth, w_[0], m_[0], v_[0], "adamw_" + nm)
            big[nm] = [r_[None] for r_ in res]
        return big[groups[group][-1]][1]

    token_in = in_flight["in"][3]
    done_ffn = finish("ffn", [token_in])
    done_mix = finish("mix", [done_ffn])

    total, g_b_mod, g_lb_full, g_w_mod, g_c_ctx = (small[k] for k in ("total", "g_b_mod", "g_lb_full", "g_w_mod",
                                                                      "g_c_ctx"))
    g_pre1, g_post1, g_pre2, g_post2 = (total[r_:r_ + 1] for r_ in (ROW_PRE1, ROW_POST1, ROW_PRE2, ROW_POST2))
    g_hg_on, g_gla_on = total[ROW_ONORM:ROW_ONORM + 1, 0:HD], total[ROW_ONORM:ROW_ONORM + 1, HD:2 * HD]
    n_lb = hg_lb.shape[2]
    g_hg_lb = lax.dynamic_slice(g_lb_full, (0, chip * n_lb), (4, n_lb))
    g_bgk = lax.dynamic_slice(total[ROW_BGK:ROW_BGK + 1].reshape(2, HW), (0, chip * n_lb), (2, n_lb))
    g_wgk_full = total[ROW_WGK:ROW_WGK + RANK].reshape(RANK, 2, HW).transpose(1, 0, 2).reshape(2 * RANK, HW)
    g_wgk = lax.dynamic_slice(g_wgk_full, (0, chip * n_lb), (2 * RANK, n_lb))

    small_items = [
        (g_c_ctx, c_ctx.reshape(1, d), m_c_ctx.reshape(1, d), v_c_ctx.reshape(1, d)),
        (g_b_mod, b_mod, m_b_mod, v_b_mod),
        (g_pre1, norm_pre1, m_norm_pre1, v_norm_pre1),
        (g_post1, norm_post1, m_norm_post1, v_norm_post1),
        (g_pre2, norm_pre2, m_norm_pre2, v_norm_pre2),
        (g_post2, norm_post2, m_norm_post2, v_norm_post2),
        (g_hg_lb, hg_lb.reshape(4, n_lb), m_hg_lb.reshape(4, n_lb), v_hg_lb.reshape(4, n_lb)),
        (g_hg_on, hg_onorm, m_hg_onorm, v_hg_onorm),
        (g_wgk, gla_w_gk.reshape(2 * RANK, n_lb), m_gla_w_gk.reshape(2 * RANK, n_lb), v_gla_w_gk.reshape(2 * RANK, n_lb)),
        (g_bgk, gla_b_gk.reshape(2, n_lb), m_gla_b_gk.reshape(2, n_lb), v_gla_b_gk.reshape(2, n_lb)),
        (g_gla_on, gla_onorm, m_gla_onorm, v_gla_onorm),
    ]
    small_res = _adamw_whole(small_items, "adamw_small")
    mod_res = _adamw_tiled(g_w_mod, w_mod[0], m_w_mod[0], v_w_mod[0], "adamw_w_mod")
    finish("in", [done_mix, mod_res[0], small_res[0][0]])

    loss = lax.psum(loss_part[0, 0], ("x", "y", "c"))

    shapes = dict(c_ctx=c_ctx.shape, b_mod=b_mod.shape, norm_pre1=norm_pre1.shape, norm_post1=norm_post1.shape,
                  norm_pre2=norm_pre2.shape, norm_post2=norm_post2.shape, hg_lb=hg_lb.shape, hg_onorm=hg_onorm.shape,
                  gla_w_gk=gla_w_gk.shape, gla_b_gk=gla_b_gk.shape, gla_onorm=gla_onorm.shape)
    small_names = ["c_ctx", "b_mod", "norm_pre1", "norm_post1", "norm_pre2", "norm_post2", "hg_lb", "hg_onorm",
                   "gla_w_gk", "gla_b_gk", "gla_onorm"]
    grads, deltas, new_m, new_v = {}, {}, {}, {}
    for nm, item, res in zip(small_names, small_items, small_res):
        grads[nm] = item[0].reshape(shapes[nm])
        deltas[nm], new_m[nm], new_v[nm] = (r.reshape(shapes[nm]) for r in res)
    grads["w_mod"] = g_w_mod[None]
    deltas["w_mod"], new_m["w_mod"], new_v["w_mod"] = (r[None] for r in mod_res)
    for nm in names:
        grads[nm], deltas[nm], new_m[nm], new_v[nm] = big[nm]
    order = ["c_ctx", "w_mod", "b_mod", "norm_pre1", "norm_post1", "norm_pre2", "norm_post2", "w_in", "hg_lb",
             "hg_onorm", "gla_w_gk", "gla_b_gk", "gla_onorm", "w_br_hg", "w_br_gla", "w_out", "w_ff_gate", "w_ff_up",
             "w_ff_down"]
    return (loss, grad_x[None], *[grads[n] for n in order], *[deltas[n] for n in order],
            *[new_m[n] for n in order], *[new_v[n] for n in order])


def _weight_grad_cols(xs, dy, n_cols, name, tk=256, tn=512):
    rows = dy.shape[0]
    k = xs.shape[1]

    def body(x_ref, dy_ref, o_ref):
        o_ref[...] = _dot_tn(x_ref[...], dy_ref[...])

    return pl.pallas_call(
        body, name=name, grid=(k // tk, n_cols // tn),
        out_shape=jax.ShapeDtypeStruct((k, n_cols), F32),
        in_specs=[pl.BlockSpec((rows, tk), lambda i, j: (0, i)), pl.BlockSpec((rows, tn), lambda i, j: (0, j))],
        out_specs=pl.BlockSpec((tk, tn), lambda i, j: (i, j)),
        compiler_params=_cparams(dimension_semantics=("parallel", "parallel")),
    )(xs, dy)
```

```python
import functools

import jax
import jax.numpy as jnp
from jax import lax
from jax.experimental import pallas as pl
from jax.experimental.pallas import tpu as pltpu

F32 = jnp.float32
BF16 = jnp.bfloat16
HIGHEST = lax.Precision.HIGHEST
MESH = pl.DeviceIdType.MESH

EPS = 1e-6
CHUNK = 64
SUB = 16
NSUB = CHUNK // SUB
NH = 4
HD = 128
HW = NH * HD
RANK = 16
GATE_NORM = 16.0
N_MOD = 6
TM = 256
TM_FFN = 128
N_DEV = 8
N_CHIP = 4
VMEM_LIMIT = 56 * 1024 * 1024

ADAM_LR = 0.001
ADAM_B1 = 0.9
ADAM_B2 = 0.999
ADAM_EPS = 1e-08
ADAM_WD = 0.01
ADAM_STEP = 10

VMEM_SPEC = pl.BlockSpec(memory_space=pltpu.VMEM)
ANY_SPEC = pl.BlockSpec(memory_space=pl.ANY)
HBM_SPEC = pl.BlockSpec(memory_space=pltpu.HBM)
SEM_SPEC = pl.BlockSpec(memory_space=pltpu.SEMAPHORE)
EFFECT = pltpu.SideEffectType.DATAFLOW_SIDE_EFFECTING


def _cparams(**kw):
    return pltpu.CompilerParams(vmem_limit_bytes=VMEM_LIMIT, **kw)


def _dot(a, b):
    return jnp.dot(a.astype(BF16), b.astype(BF16), preferred_element_type=F32)


def _dot_nt(a, b):
    return lax.dot_general(a.astype(BF16), b.astype(BF16), (((1,), (1,)), ((), ())), preferred_element_type=F32)


def _dot_tn(a, b):
    return lax.dot_general(a.astype(BF16), b.astype(BF16), (((0,), (0,)), ((), ())), preferred_element_type=F32)


def _sigmoid(x):
    return 1.0 / (1.0 + jnp.exp(-x))


def _silu(x):
    return x * _sigmoid(x)


def _dsilu(x):
    s = _sigmoid(x)
    return s * (1.0 + x * (1.0 - s))


def _log_sigmoid(x):
    return jnp.minimum(x, 0.0) - jnp.log(1.0 + jnp.exp(-jnp.abs(x)))


def _colsum(a):
    return jnp.sum(a, axis=0, keepdims=True)


def _rms(a):
    r = lax.rsqrt(jnp.mean(a * a, axis=-1, keepdims=True) + EPS)
    return a * r, r


def _rms_bwd(dn, n, r):
    return r * (dn - n * jnp.mean(dn * n, axis=-1, keepdims=True))


def _place():
    x, y, c = lax.axis_index("x"), lax.axis_index("y"), lax.axis_index("c")
    chips = [(1 - x, y), (x, 1 - y), (1 - x, 1 - y)]
    return x, y, c, chips


def _allgather8(v, name):
    rows, cols = v.shape

    def body(x_ref, out_ref, send_sems, recv_sems, local_sem):
        x, y, c, chips = _place()
        me, sibling = (x, y, c), (x, y, 1 - c)

        def blk(px, py, pc):
            return out_ref.at[4 * px + 2 * py + pc]

        def copy(k, block, to, src=None):
            return pltpu.make_async_remote_copy(
                src_ref=blk(*block) if src is None else src, dst_ref=blk(*block),
                send_sem=send_sems.at[k], recv_sem=recv_sems.at[k], device_id=to, device_id_type=MESH)

        mine = pltpu.make_async_copy(x_ref, blk(*me), local_sem)
        mine.start()
        first = [copy(0, me, sibling, src=x_ref)]
        first += [copy(1 + j, me, (*chip, c), src=x_ref) for j, chip in enumerate(chips)]
        for cp in first:
            cp.start()
        passed = [copy(4 + j, (*chip, c), sibling) for j, chip in enumerate(chips)]
        for j, chip in enumerate(chips):
            copy(1 + j, (*chip, c), me).wait_recv()
            passed[j].start()
        copy(0, sibling, me).wait_recv()
        for j, chip in enumerate(chips):
            copy(4 + j, (*chip, 1 - c), me).wait_recv()
        for cp in first + passed:
            cp.wait_send()
        mine.wait()

    return pl.pallas_call(
        body, name=name,
        out_shape=jax.ShapeDtypeStruct((N_DEV, rows, cols), v.dtype),
        in_specs=[VMEM_SPEC], out_specs=VMEM_SPEC,
        scratch_shapes=[pltpu.SemaphoreType.DMA((7,)), pltpu.SemaphoreType.DMA((7,)), pltpu.SemaphoreType.DMA],
    )(v)


def _cast_into_blocks(chip_arr, w, name):
    rows, cols = w.shape
    tr = _row_tile(rows, 16, 256)

    def body(chip_ref, w_ref, o_ref):
        o_ref[0] = w_ref[...].astype(BF16)

    return pl.pallas_call(
        body, name=name,
        grid_spec=pltpu.PrefetchScalarGridSpec(
            num_scalar_prefetch=1, grid=(rows // tr,),
            in_specs=[pl.BlockSpec((tr, cols), lambda i, chip_ref: (i, 0))],
            out_specs=pl.BlockSpec((1, tr, cols), lambda i, chip_ref: (chip_ref[0], i, 0))),
        out_shape=jax.ShapeDtypeStruct((N_CHIP, rows, cols), BF16),
        compiler_params=_cparams(dimension_semantics=("parallel",)),
    )(chip_arr, w)


def _half_rows(ref, chip_id, pc):
    h = ref.shape[1] // 2
    return ref.at[chip_id, pl.ds(pl.multiple_of(pc * h, 8), h), :]


def _gather_blocks(lands, name):
    n = len(lands)

    def body(*refs):
        outs = refs[n:2 * n]
        send_sems, recv_sems = refs[2 * n:]
        x, y, c, chips = _place()
        me_chip = 2 * x + y
        sibling = (x, y, 1 - c)

        def copy(k, j, chip_id, pc, to):
            return pltpu.make_async_remote_copy(
                src_ref=_half_rows(outs[k], chip_id, pc), dst_ref=_half_rows(outs[k], chip_id, pc),
                send_sem=send_sems.at[k, j], recv_sem=recv_sems.at[k, j], device_id=to, device_id_type=MESH)

        started = []
        for k in range(n):
            for j, chip in enumerate(chips):
                cp = copy(k, j, me_chip, c, (*chip, c))
                cp.start()
                started.append(cp)
        for k in range(n):
            for j, (px, py) in enumerate(chips):
                copy(k, j, 2 * px + py, c, sibling).wait_recv()
                cp = copy(k, 3 + j, 2 * px + py, c, sibling)
                cp.start()
                started.append(cp)
        for k in range(n):
            for j, (px, py) in enumerate(chips):
                copy(k, 3 + j, 2 * px + py, 1 - c, sibling).wait_recv()
        for cp in started:
            cp.wait_send()

    return pl.pallas_call(
        body, name=name,
        out_shape=[jax.ShapeDtypeStruct(l.shape, l.dtype) for l in lands],
        in_specs=[ANY_SPEC] * n, out_specs=[ANY_SPEC] * n,
        input_output_aliases={i: i for i in range(n)},
        scratch_shapes=[pltpu.SemaphoreType.DMA((n, 6)), pltpu.SemaphoreType.DMA((n, 6))],
    )(*lands)


def _hbm(a):
    return pltpu.with_memory_space_constraint(a, pltpu.HBM)


def _blocks_start(lands, name):
    n = len(lands)
    n_sem = 3 * n

    def body(*refs):
        lnd = refs[:n]
        send_sems, recv_sems = refs[n:n + n_sem], refs[n + n_sem:n + 2 * n_sem]
        token = refs[-1]
        x, y, c, chips = _place()
        me_chip = 2 * x + y
        for k in range(n):
            for j, chip in enumerate(chips):
                pltpu.make_async_remote_copy(
                    src_ref=_half_rows(lnd[k], me_chip, c), dst_ref=_half_rows(lnd[k], me_chip, c),
                    send_sem=send_sems[3 * k + j], recv_sem=recv_sems[3 * k + j],
                    device_id=(*chip, c), device_id_type=MESH).start()
        token[...] = jnp.zeros_like(token)

    out = pl.pallas_call(
        body, name=name,
        out_shape=(*[pltpu.SemaphoreType.DMA(())] * (2 * n_sem),
                   *[pltpu.HBM(l.shape, l.dtype) for l in lands],
                   jax.ShapeDtypeStruct((8, 128), F32)),
        in_specs=[HBM_SPEC] * n,
        out_specs=(*[SEM_SPEC] * (2 * n_sem), *[HBM_SPEC] * n, VMEM_SPEC),
        input_output_aliases={i: 2 * n_sem + i for i in range(n)},
        compiler_params=pltpu.CompilerParams(has_side_effects=EFFECT),
    )(*[_hbm(l) for l in lands])
    return list(out[:2 * n_sem]), list(out[2 * n_sem:2 * n_sem + n]), out[-1]


def _blocks_wait(sems, lands, after, name):
    n = len(lands)
    n_sem = 3 * n

    def body(*refs):
        lnd = refs[:n]
        s_sems, r_sems = refs[n:n + n_sem], refs[n + n_sem:n + 2 * n_sem]
        x, y, c, chips = _place()
        me_chip = 2 * x + y
        for k in range(n):
            for j, (px, py) in enumerate(chips):
                cp = pltpu.make_async_remote_copy(
                    src_ref=_half_rows(lnd[k], me_chip, c), dst_ref=_half_rows(lnd[k], 2 * px + py, c),
                    send_sem=s_sems[3 * k + j], recv_sem=r_sems[3 * k + j],
                    device_id=(px, py, c), device_id_type=MESH)
                cp.wait_send()
                cp.wait_recv()

    out = pl.pallas_call(
        body, name=name,
        out_shape=tuple(pltpu.HBM(l.shape, l.dtype) for l in lands),
        in_specs=[HBM_SPEC] * n + [SEM_SPEC] * (2 * n_sem) + [ANY_SPEC],
        out_specs=[HBM_SPEC] * n,
        input_output_aliases={i: i for i in range(n)},
        compiler_params=pltpu.CompilerParams(has_side_effects=EFFECT),
    )(*lands, *sems, after)
    return list(out)


def _blocks_finish(lands, name):
    n = len(lands)

    def body(*refs):
        lnd = refs[n:2 * n]
        send_sems, recv_sems = refs[2 * n:]
        x, y, c, chips = _place()
        sibling = (x, y, 1 - c)

        def copy(k, j, chip_id, pc):
            return pltpu.make_async_remote_copy(
                src_ref=_half_rows(lnd[k], chip_id, pc), dst_ref=_half_rows(lnd[k], chip_id, pc),
                send_sem=send_sems.at[k, j], recv_sem=recv_sems.at[k, j], device_id=sibling, device_id_type=MESH)

        started = []
        for k in range(n):
            for j, (px, py) in enumerate(chips):
                cp = copy(k, j, 2 * px + py, c)
                cp.start()
                started.append(cp)
        for k in range(n):
            for j, (px, py) in enumerate(chips):
                copy(k, j, 2 * px + py, 1 - c).wait_recv()
        for cp in started:
            cp.wait_send()

    out = pl.pallas_call(
        body, name=name,
        out_shape=[jax.ShapeDtypeStruct(l.shape, l.dtype) for l in lands],
        in_specs=[ANY_SPEC] * n, out_specs=[ANY_SPEC] * n,
        input_output_aliases={i: i for i in range(n)},
        scratch_shapes=[pltpu.SemaphoreType.DMA((n, 3)), pltpu.SemaphoreType.DMA((n, 3))],
    )(*lands)
    return list(out)


def _gather_start(shards, name):
    n = len(shards)
    n_sem = 3 * n

    def body(*refs):
        ins, lands = refs[:n], refs[n:2 * n]
        send_sems, recv_sems = refs[2 * n:2 * n + n_sem], refs[2 * n + n_sem:2 * n + 2 * n_sem]
        token = refs[-1]
        x, y, c, chips = _place()
        me_chip = 2 * x + y
        for k in range(n):
            h = shards[k].shape[0] // 2
            rows = pl.ds(pl.multiple_of(c * h, 8), h)
            for j, chip in enumerate(chips):
                pltpu.make_async_remote_copy(
                    src_ref=ins[k].at[rows, :], dst_ref=lands[k].at[me_chip, rows, :],
                    send_sem=send_sems[3 * k + j], recv_sem=recv_sems[3 * k + j],
                    device_id=(*chip, c), device_id_type=MESH).start()
        token[...] = jnp.zeros_like(token)

    lands = [_hbm(lax.empty((N_CHIP,) + s.shape, s.dtype)) for s in shards]
    out = pl.pallas_call(
        body, name=name,
        out_shape=(*[pltpu.SemaphoreType.DMA(())] * (2 * n_sem),
                   *[pltpu.HBM(s.shape, s.dtype) for s in shards],
                   *[pltpu.HBM(l.shape, l.dtype) for l in lands],
                   jax.ShapeDtypeStruct((8, 128), F32)),
        in_specs=[HBM_SPEC] * (2 * n),
        out_specs=(*[SEM_SPEC] * (2 * n_sem), *[HBM_SPEC] * (2 * n), VMEM_SPEC),
        input_output_aliases={i: 2 * n_sem + i for i in range(2 * n)},
        compiler_params=pltpu.CompilerParams(has_side_effects=EFFECT),
    )(*[_hbm(s) for s in shards], *lands)
    sems = list(out[:2 * n_sem])
    return sems, list(out[2 * n_sem:2 * n_sem + n]), list(out[2 * n_sem + n:2 * n_sem + 2 * n]), out[-1]


def _gather_wait(sems, shards, lands, after, name):
    n = len(shards)
    n_sem = 3 * n

    def body(*refs):
        ins, lnd = refs[:n], refs[n:2 * n]
        s_sems, r_sems = refs[2 * n:2 * n + n_sem], refs[2 * n + n_sem:2 * n + 2 * n_sem]
        x, y, c, chips = _place()
        for k in range(n):
            h = shards[k].shape[0] // 2
            rows = pl.ds(pl.multiple_of(c * h, 8), h)
            for j, (px, py) in enumerate(chips):
                cp = pltpu.make_async_remote_copy(
                    src_ref=ins[k].at[rows, :], dst_ref=lnd[k].at[2 * px + py, rows, :],
                    send_sem=s_sems[3 * k + j], recv_sem=r_sems[3 * k + j],
                    device_id=(px, py, c), device_id_type=MESH)
                cp.wait_send()
                cp.wait_recv()

    out = pl.pallas_call(
        body, name=name,
        out_shape=(*[pltpu.HBM(s.shape, s.dtype) for s in shards], *[pltpu.HBM(l.shape, l.dtype) for l in lands]),
        in_specs=[HBM_SPEC] * (2 * n) + [SEM_SPEC] * (2 * n_sem) + [ANY_SPEC],
        out_specs=[HBM_SPEC] * (2 * n),
        input_output_aliases={i: i for i in range(2 * n)},
        compiler_params=pltpu.CompilerParams(has_side_effects=EFFECT),
    )(*shards, *lands, *sems, after)
    return list(out[:n]), list(out[n:])


def _gather_finish(shards, lands, name):
    n = len(shards)

    def body(*refs):
        ins, lnd = refs[:n], refs[2 * n:3 * n]
        send_sems, recv_sems, local_sems = refs[3 * n:]
        x, y, c, chips = _place()
        me_chip = 2 * x + y
        sibling = (x, y, 1 - c)

        def half(k, chip_id, pc):
            h = shards[k].shape[0] // 2
            return lnd[k].at[chip_id, pl.ds(pl.multiple_of(pc * h, 8), h), :]

        def copy(k, j, chip_id, pc):
            return pltpu.make_async_remote_copy(
                src_ref=half(k, chip_id, pc), dst_ref=half(k, chip_id, pc),
                send_sem=send_sems.at[k, j], recv_sem=recv_sems.at[k, j], device_id=sibling, device_id_type=MESH)

        locals_, started = [], []
        for k in range(n):
            cp = pltpu.make_async_copy(ins[k], lnd[k].at[me_chip], local_sems.at[k])
            cp.start()
            locals_.append(cp)
            for j, (px, py) in enumerate(chips):
                cp = copy(k, j, 2 * px + py, c)
                cp.start()
                started.append(cp)
        for k in range(n):
            for j, (px, py) in enumerate(chips):
                copy(k, j, 2 * px + py, 1 - c).wait_recv()
        for cp in started:
            cp.wait_send()
        for cp in locals_:
            cp.wait()

    out = pl.pallas_call(
        body, name=name,
        out_shape=[jax.ShapeDtypeStruct(l.shape, l.dtype) for l in lands],
        in_specs=[ANY_SPEC] * (2 * n), out_specs=[ANY_SPEC] * n,
        input_output_aliases={n + i: i for i in range(n)},
        scratch_shapes=[pltpu.SemaphoreType.DMA((n, 3)), pltpu.SemaphoreType.DMA((n, 3)),
                        pltpu.SemaphoreType.DMA((n,))],
    )(*shards, *lands)
    return list(out)


def _send_other_half(arrs, name):
    n = len(arrs)

    def body(*refs):
        ins, outs = refs[:n], refs[n:2 * n]
        send_sems, recv_sems = refs[2 * n:]
        x, y, c, _ = _place()
        cps = []
        for k in range(n):
            h = arrs[k].shape[1] // 2
            cp = pltpu.make_async_remote_copy(
                src_ref=ins[k].at[:, pl.ds(pl.multiple_of((1 - c) * h, 8), h), :], dst_ref=outs[k],
                send_sem=send_sems.at[k], recv_sem=recv_sems.at[k], device_id=(x, y, 1 - c), device_id_type=MESH)
            cp.start()
            cps.append(cp)
        for cp in cps:
            cp.wait()

    return pl.pallas_call(
        body, name=name,
        out_shape=[jax.ShapeDtypeStruct((a.shape[0], a.shape[1] // 2, a.shape[2]), a.dtype) for a in arrs],
        in_specs=[ANY_SPEC] * n, out_specs=[ANY_SPEC] * n,
        scratch_shapes=[pltpu.SemaphoreType.DMA((n,)), pltpu.SemaphoreType.DMA((n,))],
    )(*arrs)


def _blocks_to_owner(arrs, name):
    n = len(arrs)

    def body(*refs):
        ins, outs = refs[:n], refs[n:2 * n]
        send_sems, recv_sems, local_sems = refs[2 * n:]
        x, y, c, chips = _place()
        me_chip = 2 * x + y
        locals_, started = [], []
        for k in range(n):
            cp = pltpu.make_async_copy(ins[k].at[me_chip], outs[k].at[me_chip], local_sems.at[k])
            cp.start()
            locals_.append(cp)

        def copy(k, j, src_block, dst_slot, to):
            return pltpu.make_async_remote_copy(
                src_ref=ins[k].at[src_block], dst_ref=outs[k].at[dst_slot],
                send_sem=send_sems.at[k, j], recv_sem=recv_sems.at[k, j], device_id=to, device_id_type=MESH)

        for k in range(n):
            for j, (px, py) in enumerate(chips):
                cp = copy(k, j, 2 * px + py, me_chip, (px, py, c))
                cp.start()
                started.append(cp)
        for k in range(n):
            for j, (px, py) in enumerate(chips):
                copy(k, j, me_chip, 2 * px + py, (px, py, c)).wait_recv()
        for cp in started:
            cp.wait_send()
        for cp in locals_:
            cp.wait()

    return pl.pallas_call(
        body, name=name,
        out_shape=[jax.ShapeDtypeStruct(a.shape, a.dtype) for a in arrs],
        in_specs=[ANY_SPEC] * n, out_specs=[ANY_SPEC] * n,
        scratch_shapes=[pltpu.SemaphoreType.DMA((n, 3)), pltpu.SemaphoreType.DMA((n, 3)),
                        pltpu.SemaphoreType.DMA((n,))],
    )(*arrs)


def _scatter_blocks(arrs, name):
    n = len(arrs)

    def body(*refs):
        ins, outs = refs[:n], refs[n:2 * n]
        send_sems, recv_sems = refs[2 * n:]
        x, y, c, chips = _place()
        me_chip = 2 * x + y

        def copy(k, j, src_block, dst_slot, to):
            return pltpu.make_async_remote_copy(
                src_ref=ins[k].at[src_block], dst_ref=outs[k].at[dst_slot],
                send_sem=send_sems.at[k, j], recv_sem=recv_sems.at[k, j], device_id=to, device_id_type=MESH)

        started = []
        for k in range(n):
            for j, (px, py) in enumerate(chips):
                cp = copy(k, j, 2 * px + py, me_chip, (px, py, c))
                cp.start()
                started.append(cp)
        for k in range(n):
            for j, (px, py) in enumerate(chips):
                copy(k, j, me_chip, 2 * px + py, (px, py, c)).wait_recv()
        for cp in started:
            cp.wait_send()

    return pl.pallas_call(
        body, name=name,
        out_shape=[jax.ShapeDtypeStruct(a.shape, a.dtype) for a in arrs],
        in_specs=[ANY_SPEC] * n, out_specs=[ANY_SPEC] * n,
        scratch_shapes=[pltpu.SemaphoreType.DMA((n, 3)), pltpu.SemaphoreType.DMA((n, 3))],
    )(*arrs)


def _scatter_start(arrs, name, after=()):
    n = len(arrs)
    n_sem = 3 * n
    first = 2 * n + len(after)

    def body(*refs):
        ins, lnd = refs[:n], refs[n:2 * n]
        send_sems, recv_sems = refs[first:first + n_sem], refs[first + n_sem:first + 2 * n_sem]
        token = refs[-1]
        x, y, c, chips = _place()
        me_chip = 2 * x + y
        for k in range(n):
            for j, (px, py) in enumerate(chips):
                pltpu.make_async_remote_copy(
                    src_ref=ins[k].at[2 * px + py], dst_ref=lnd[k].at[me_chip],
                    send_sem=send_sems[3 * k + j], recv_sem=recv_sems[3 * k + j],
                    device_id=(px, py, c), device_id_type=MESH).start()
        token[...] = jnp.zeros_like(token)

    lands = [_hbm(lax.empty(a.shape, a.dtype)) for a in arrs]
    out = pl.pallas_call(
        body, name=name,
        out_shape=(*[pltpu.SemaphoreType.DMA(())] * (2 * n_sem),
                   *[pltpu.HBM(a.shape, a.dtype) for a in arrs], *[pltpu.HBM(a.shape, a.dtype) for a in arrs],
                   jax.ShapeDtypeStruct((8, 128), F32)),
        in_specs=[HBM_SPEC] * (2 * n) + [ANY_SPEC] * len(after),
        out_specs=(*[SEM_SPEC] * (2 * n_sem), *[HBM_SPEC] * (2 * n), VMEM_SPEC),
        input_output_aliases={i: 2 * n_sem + i for i in range(2 * n)},
        compiler_params=pltpu.CompilerParams(has_side_effects=EFFECT),
    )(*[_hbm(a) for a in arrs], *lands, *after)
    base = 2 * n_sem
    return list(out[:base]), list(out[base:base + n]), list(out[base + n:base + 2 * n]), out[-1]


def _scatter_wait(sems, arrs, lands, after, name):
    n = len(arrs)
    n_sem = 3 * n

    def body(*refs):
        ins, lnd = refs[:n], refs[n:2 * n]
        s_sems, r_sems = refs[2 * n:2 * n + n_sem], refs[2 * n + n_sem:2 * n + 2 * n_sem]
        x, y, c, chips = _place()
        for k in range(n):
            for j, (px, py) in enumerate(chips):
                cp = pltpu.make_async_remote_copy(
                    src_ref=ins[k].at[2 * px + py], dst_ref=lnd[k].at[2 * px + py],
                    send_sem=s_sems[3 * k + j], recv_sem=r_sems[3 * k + j],
                    device_id=(px, py, c), device_id_type=MESH)
                cp.wait_send()
                cp.wait_recv()

    out = pl.pallas_call(
        body, name=name,
        out_shape=tuple(pltpu.HBM(a.shape, a.dtype) for a in list(arrs) + list(lands)),
        in_specs=[HBM_SPEC] * (2 * n) + [SEM_SPEC] * (2 * n_sem) + [ANY_SPEC] * len(after),
        out_specs=[HBM_SPEC] * (2 * n),
        input_output_aliases={i: i for i in range(2 * n)},
        compiler_params=pltpu.CompilerParams(has_side_effects=EFFECT),
    )(*arrs, *lands, *sems, *after)
    return list(out[:n]), list(out[n:])


def _sum_owner(chip_arr, pairs, got, name):
    nb, h, cols = got.shape
    tr = _row_tile(h, 16, 256)

    def body(chip_ref, own_ref, a_ref, b_ref, c_ref, o_ref):
        o_ref[...] = ((own_ref[0].astype(F32) + a_ref[0].astype(F32)) + b_ref[0].astype(F32)) + c_ref[0].astype(F32)

    def slot(off):
        return pl.BlockSpec((1, tr, cols), lambda i, chip_ref: ((chip_ref[0] + off) % N_CHIP, i, 0))

    return pl.pallas_call(
        body, name=name,
        grid_spec=pltpu.PrefetchScalarGridSpec(
            num_scalar_prefetch=1, grid=(h // tr,),
            in_specs=[slot(0), slot(1), slot(2), slot(3)],
            out_specs=pl.BlockSpec((tr, cols), lambda i, chip_ref: (i, 0))),
        out_shape=jax.ShapeDtypeStruct((h, cols), F32),
        compiler_params=_cparams(dimension_semantics=("parallel",)),
    )(chip_arr, pairs, got, got, got)


def _swap_with_sibling(arrs, name):
    n = len(arrs)

    def body(*refs):
        ins, outs = refs[:n], refs[n:2 * n]
        send_sems, recv_sems = refs[2 * n:]
        x, y, c, _ = _place()
        cps = []
        for k in range(n):
            cp = pltpu.make_async_remote_copy(
                src_ref=ins[k], dst_ref=outs[k], send_sem=send_sems.at[k], recv_sem=recv_sems.at[k],
                device_id=(x, y, 1 - c), device_id_type=MESH)
            cp.start()
            cps.append(cp)
        for cp in cps:
            cp.wait()

    return pl.pallas_call(
        body, name=name,
        out_shape=[jax.ShapeDtypeStruct(a.shape, a.dtype) for a in arrs],
        in_specs=[ANY_SPEC] * n, out_specs=[ANY_SPEC] * n,
        scratch_shapes=[pltpu.SemaphoreType.DMA((n,)), pltpu.SemaphoreType.DMA((n,))],
    )(*arrs)


def _row_tile(h, mult=8, cap=128):
    for t in range(cap - cap % mult, mult - 1, -mult):
        if h % t == 0:
            return t
    raise ValueError(h)


def _cast_bf16(a, name):
    rows, cols = a.shape
    tr = _row_tile(rows, 16, 256)

    def body(a_ref, o_ref):
        o_ref[...] = a_ref[...].astype(BF16)

    return pl.pallas_call(
        body, name=name, grid=(rows // tr,),
        out_shape=jax.ShapeDtypeStruct(a.shape, BF16),
        in_specs=[pl.BlockSpec((tr, cols), lambda i: (i, 0))],
        out_specs=pl.BlockSpec((tr, cols), lambda i: (i, 0)),
        compiler_params=_cparams(dimension_semantics=("parallel",)),
    )(a)


def _pair_sum(c_arr, full, recv, name):
    nb, rows, cols = full.shape
    h = rows // 2
    tr = _row_tile(h, 16, 256)
    steps = h // tr

    def body(c_ref, f_ref, r_ref, o_ref):
        o_ref[...] = (f_ref[...] + r_ref[...]).astype(BF16)

    return pl.pallas_call(
        body, name=name,
        grid_spec=pltpu.PrefetchScalarGridSpec(
            num_scalar_prefetch=1, grid=(nb, steps),
            in_specs=[pl.BlockSpec((1, tr, cols), lambda b, i, c_ref: (b, c_ref[0] * steps + i, 0)),
                      pl.BlockSpec((1, tr, cols), lambda b, i, c_ref: (b, i, 0))],
            out_specs=pl.BlockSpec((1, tr, cols), lambda b, i, c_ref: (b, i, 0))),
        out_shape=jax.ShapeDtypeStruct((nb, h, cols), BF16),
        compiler_params=_cparams(dimension_semantics=("parallel", "parallel")),
    )(c_arr, full, recv)


def _sum_chips(got, name):
    nb, h, cols = got.shape
    tr = _row_tile(h, 16, 256)

    def body(g_ref, o_ref):
        g = g_ref[...].astype(F32)
        o_ref[...] = ((g[0] + g[1]) + g[2]) + g[3]

    return pl.pallas_call(
        body, name=name, grid=(h // tr,),
        out_shape=jax.ShapeDtypeStruct((h, cols), F32),
        in_specs=[pl.BlockSpec((nb, tr, cols), lambda i: (0, i, 0))],
        out_specs=pl.BlockSpec((tr, cols), lambda i: (i, 0)),
        compiler_params=_cparams(dimension_semantics=("parallel",)),
    )(got)


def _adam_math(g, w, m, v):
    m1 = ADAM_B1 * m + (1.0 - ADAM_B1) * g
    v1 = ADAM_B2 * v + (1.0 - ADAM_B2) * (g * g)
    m_hat = m1 / (1.0 - ADAM_B1 ** ADAM_STEP)
    v_hat = v1 / (1.0 - ADAM_B2 ** ADAM_STEP)
    delta = -ADAM_LR * (m_hat / (jnp.sqrt(v_hat) + ADAM_EPS) + ADAM_WD * w)
    return delta, m1, v1


def _adamw_halves(c_arr, own, other, w, m, v, name):
    rows, cols = w.shape
    h = rows // 2
    tr = _row_tile(h)
    steps = h // tr

    def body(c_ref, own_ref, oth_ref, w_ref, m_ref, v_ref, g_out, d_out, m_out, v_out):
        g = jnp.where(pl.program_id(0) == c_ref[0], own_ref[...], oth_ref[...])
        d, m1, v1 = _adam_math(g, w_ref[...], m_ref[...], v_ref[...])
        g_out[...] = g
        d_out[...] = d
        m_out[...] = m1
        v_out[...] = v1

    half_spec = pl.BlockSpec((tr, cols), lambda p, i, c_ref: (i, 0))
    full_spec = pl.BlockSpec((tr, cols), lambda p, i, c_ref: (p * steps + i, 0))
    return pl.pallas_call(
        body, name=name,
        grid_spec=pltpu.PrefetchScalarGridSpec(
            num_scalar_prefetch=1, grid=(2, steps),
            in_specs=[half_spec, half_spec, full_spec, full_spec, full_spec],
            out_specs=[full_spec] * 4),
        out_shape=[jax.ShapeDtypeStruct(w.shape, F32)] * 4,
        compiler_params=_cparams(dimension_semantics=("parallel", "parallel")),
    )(c_arr, own, other, w, m, v)


def _adamw_whole(items, name):
    n = len(items)

    def body(*refs):
        ins, outs = refs[:4 * n], refs[4 * n:]
        for k in range(n):
            g, w, m, v = (r[...] for r in ins[4 * k:4 * k + 4])
            d, m1, v1 = _adam_math(g, w, m, v)
            outs[3 * k][...] = d
            outs[3 * k + 1][...] = m1
            outs[3 * k + 2][...] = v1

    flat = [a for it in items for a in it]
    shapes = [jax.ShapeDtypeStruct(it[1].shape, F32) for it in items for _ in range(3)]
    out = pl.pallas_call(
        body, name=name, out_shape=shapes,
        in_specs=[VMEM_SPEC] * (4 * n), out_specs=[VMEM_SPEC] * (3 * n),
        compiler_params=_cparams(),
    )(*flat)
    return [tuple(out[3 * k:3 * k + 3]) for k in range(n)]


def _adamw_tiled(g, w, m, v, name):
    rows, cols = w.shape
    tr = _row_tile(rows)

    def body(g_ref, w_ref, m_ref, v_ref, d_out, m_out, v_out):
        d, m1, v1 = _adam_math(g_ref[...], w_ref[...], m_ref[...], v_ref[...])
        d_out[...] = d
        m_out[...] = m1
        v_out[...] = v1

    spec = pl.BlockSpec((tr, cols), lambda i: (i, 0))
    return pl.pallas_call(
        body, name=name, grid=(rows // tr,),
        out_shape=[jax.ShapeDtypeStruct(w.shape, F32)] * 3,
        in_specs=[spec] * 4, out_specs=[spec] * 3,
        compiler_params=_cparams(dimension_semantics=("parallel",)),
    )(g, w, m, v)


def _mod_forward(cond, w_mod, b_mod_cols, name):
    def body(c_ref, w_ref, b_ref, o_ref):
        o_ref[...] = _dot(_silu(c_ref[...]), w_ref[...]) + b_ref[...]

    return pl.pallas_call(
        body, name=name, out_shape=jax.ShapeDtypeStruct((cond.shape[0], w_mod.shape[1]), F32),
        in_specs=[VMEM_SPEC] * 3, out_specs=VMEM_SPEC, compiler_params=_cparams(),
    )(cond, w_mod, b_mod_cols)


def _mod_backward(cond, w_mod, dmod_cols, name):
    def body(c_ref, w_ref, d_ref, gw_ref, gc_ref):
        s = _silu(c_ref[...])
        d = d_ref[...]
        gw_ref[...] = _dot_tn(s, d)
        gc_ref[...] = _dot_nt(d[8:16, :], w_ref[...])

    return pl.pallas_call(
        body, name=name,
        out_shape=[jax.ShapeDtypeStruct(w_mod.shape, F32), jax.ShapeDtypeStruct((8, w_mod.shape[0]), F32)],
        in_specs=[VMEM_SPEC] * 3, out_specs=[VMEM_SPEC] * 2, compiler_params=_cparams(),
    )(cond, w_mod, dmod_cols)


def _col_chunks(width, step=512):
    return [(s, min(step, width - s)) for s in range(0, width, step)]


def _in_projection(z, modc, modx, pre1, w_r, n_ctx_tiles, name):
    rows, d = z.shape
    width = w_r.shape[1]

    def body(z_ref, modc_ref, modx_ref, pre_ref, w_ref, h_ref, p_ref):
        is_ctx = pl.program_id(0) < n_ctx_tiles
        n, _ = _rms(z_ref[...])
        shift = jnp.where(is_ctx, modc_ref[0:1, :], modx_ref[0:1, :])
        scale = jnp.where(is_ctx, modc_ref[1:2, :], modx_ref[1:2, :])
        h = (n * pre_ref[...] * (1.0 + scale) + shift).astype(BF16)
        h_ref[...] = h
        for s, w in _col_chunks(width):
            p_ref[:, s:s + w] = jnp.dot(h, w_ref[:, s:s + w], preferred_element_type=F32)

    row = lambda i: (i, 0)
    fixed = lambda i: (0, 0)
    return pl.pallas_call(
        body, name=name, grid=(rows // TM,),
        out_shape=[jax.ShapeDtypeStruct((rows, d), BF16), jax.ShapeDtypeStruct((rows, width), F32)],
        in_specs=[pl.BlockSpec((TM, d), row), pl.BlockSpec((8, d), fixed), pl.BlockSpec((8, d), fixed),
                  pl.BlockSpec((1, d), fixed), VMEM_SPEC],
        out_specs=[pl.BlockSpec((TM, d), row), pl.BlockSpec((TM, width), row)],
        compiler_params=_cparams(dimension_semantics=("parallel",)),
    )(z, modc, modx, pre1, w_r)


C_HQ, C_HI, C_HF_FW, C_HF_BW, C_HGATE, C_GQ, C_GK, C_GV, C_GGATE = range(9)
OFF_GATE_HG = 9 * HW
OFF_LR = 13 * HW
P_WIDTH = OFF_LR + 128


def _head_norm_fwd(o, w):
    outs, ns, rs = [], [], []
    for h in range(NH):
        n, r = _rms(o[:, h * HD:(h + 1) * HD])
        ns.append(n)
        rs.append(r)
        outs.append(n * w)
    return jnp.concatenate(outs, axis=1), ns, rs


def _mixer_tail(z, o_hg, o_gla, p_hgate, p_ggate, p_gate_hg, p_gate_gla, hg_on, gla_on, wbh, wbg, wout):
    on_hg, n_hg, r_hg = _head_norm_fwd(o_hg, hg_on)
    on_gla, n_gla, r_gla = _head_norm_fwd(o_gla, gla_on)
    og_hg = (on_hg * _silu(p_hgate)).astype(BF16)
    og_gla = (on_gla * _silu(p_ggate)).astype(BF16)
    b_hg = jnp.dot(og_hg, wbh, preferred_element_type=F32)
    b_gla = jnp.dot(og_gla, wbg, preferred_element_type=F32)
    s_hg = _sigmoid(p_gate_hg)
    s_gla = _sigmoid(p_gate_gla)
    merged = (s_hg * b_hg + s_gla * b_gla).astype(BF16)
    y1 = jnp.dot(merged, wout, preferred_element_type=F32)
    return dict(on_hg=on_hg, n_hg=n_hg, r_hg=r_hg, on_gla=on_gla, n_gla=n_gla, r_gla=r_gla, og_hg=og_hg,
                og_gla=og_gla, b_hg=b_hg, b_gla=b_gla, s_hg=s_hg, s_gla=s_gla, merged=merged, y1=y1)


def _mixer_ffn(x_lat, p, o_list, modx, norms, onorms, w_br_hg, w_br_gla, w_out, w_gate, w_up, w_down, target,
               n_ctx_tiles, name):
    rows, d = x_lat.shape
    dff = w_gate.shape[1]
    inv_d = 1.0 / d

    def body(x_ref, ofw_hg, obw_hg, ofw_gla, obw_gla, p_hgate, p_ggate, p_ghg_a, p_ghg_b, p_ggla_a, p_ggla_b,
             modx_ref, norm_ref, on_ref, wbh_ref, wbg_ref, wout_ref, wg_ref, wu_ref, wd_ref, t_ref,
             loss_ref, dz2_ref, y1_ref, mrg_ref, oghg_ref, oggla_ref, h2_ref, a_ref, du_ref, dv_ref, dy2_ref,
             stat_ref):
        i = pl.program_id(0)
        post1, pre2, post2 = norm_ref[1:2, :], norm_ref[2:3, :], norm_ref[3:4, :]
        gate1, shift2, scale2, gate2 = modx_ref[2:3, :], modx_ref[3:4, :], modx_ref[4:5, :], modx_ref[5:6, :]
        p_gate_hg = jnp.concatenate([p_ghg_a[...], p_ghg_b[...]], axis=1)
        p_gate_gla = jnp.concatenate([p_ggla_a[...], p_ggla_b[...]], axis=1)
        t = _mixer_tail(x_ref[...], ofw_hg[...] + obw_hg[...], ofw_gla[...] + obw_gla[...], p_hgate[...],
                        p_ggate[...], p_gate_hg, p_gate_gla, on_ref[0:1, 0:HD], on_ref[1:2, 0:HD],
                        wbh_ref[...], wbg_ref[...], wout_ref[...])
        y1_ref[...] = t["y1"]
        mrg_ref[...] = t["merged"]
        oghg_ref[...] = t["og_hg"]
        oggla_ref[...] = t["og_gla"]
        n1, _ = _rms(t["y1"])
        z2 = x_ref[...] + n1 * post1 * gate1
        n2, r2 = _rms(z2)
        nw2 = n2 * pre2
        h2 = (nw2 * (1.0 + scale2) + shift2).astype(BF16)
        h2_ref[...] = h2
        u = jnp.dot(h2, wg_ref[...], preferred_element_type=F32)
        v = jnp.dot(h2, wu_ref[...], preferred_element_type=F32)
        su = _silu(u)
        a = (su * v).astype(BF16)
        a_ref[...] = a
        y2 = jnp.dot(a, wd_ref[...], preferred_element_type=F32)
        n3, r3 = _rms(y2)
        z3 = z2 + n3 * post2 * gate2
        err = z3 - t_ref[...]
        part = 0.5 * inv_d * jnp.sum(err * err)
        dz3 = err * inv_d
        dgate2 = _colsum(dz3 * n3 * post2)
        tt = dz3 * gate2
        dpost2 = _colsum(tt * n3)
        dy2 = _rms_bwd(tt * post2, n3, r3).astype(BF16)
        dy2_ref[...] = dy2
        da = _dot_nt(dy2, wd_ref[...])
        du = (da * v * _dsilu(u)).astype(BF16)
        dv = (da * su).astype(BF16)
        du_ref[...] = du
        dv_ref[...] = dv
        dh2 = _dot_nt(du, wg_ref[...]) + _dot_nt(dv, wu_ref[...])
        dshift2 = _colsum(dh2)
        dscale2 = _colsum(dh2 * nw2)
        dnw2 = dh2 * (1.0 + scale2)
        dpre2 = _colsum(dnw2 * n2)
        dz2_ref[...] = dz3 + _rms_bwd(dnw2 * pre2, n2, r2)

        @pl.when(i == 0)
        def _():
            stat_ref[...] = jnp.zeros_like(stat_ref)
            loss_ref[...] = jnp.zeros_like(loss_ref)

        for r, val in enumerate((dshift2, dscale2, dgate2, dpre2, dpost2)):
            stat_ref[r:r + 1, :] += val
        loss_ref[...] += part

    tm = TM_FFN
    ctx_tiles = n_ctx_tiles * (TM // tm)
    lat = lambda i: (i, 0)
    full = lambda i: (i + ctx_tiles, 0)
    fixed = lambda i: (0, 0)

    def pcol(blk):
        return pl.BlockSpec((tm, HW), lambda i: (i + ctx_tiles, blk))

    in_specs = ([pl.BlockSpec((tm, d), lat)] + [pl.BlockSpec((tm, HW), full)] * 4
                + [pcol(C_HGATE), pcol(C_GGATE), pcol(9), pcol(10), pcol(11), pcol(12)]
                + [pl.BlockSpec((8, d), fixed), pl.BlockSpec((8, d), fixed), pl.BlockSpec((8, d), fixed)]
                + [VMEM_SPEC] * 6 + [pl.BlockSpec((tm, d), lat)])
    bf = lambda w: jax.ShapeDtypeStruct((rows, w), BF16)
    out_shape = [jax.ShapeDtypeStruct((8, 128), F32), jax.ShapeDtypeStruct((rows, d), F32),
                 jax.ShapeDtypeStruct((rows, d), F32), bf(d), bf(HW), bf(HW), bf(d), bf(dff), bf(dff), bf(dff), bf(d),
                 jax.ShapeDtypeStruct((8, d), F32)]
    out_specs = [pl.BlockSpec((8, 128), fixed), pl.BlockSpec((tm, d), lat), pl.BlockSpec((tm, d), lat),
                 pl.BlockSpec((tm, d), lat), pl.BlockSpec((tm, HW), lat), pl.BlockSpec((tm, HW), lat),
                 pl.BlockSpec((tm, d), lat), pl.BlockSpec((tm, dff), lat), pl.BlockSpec((tm, dff), lat),
                 pl.BlockSpec((tm, dff), lat), pl.BlockSpec((tm, d), lat), pl.BlockSpec((8, d), fixed)]
    return pl.pallas_call(
        body, name=name, grid=(rows // tm,), out_shape=out_shape, in_specs=in_specs, out_specs=out_specs,
        compiler_params=_cparams(dimension_semantics=("arbitrary",)),
    )(x_lat, *o_list, p, p, p, p, p, p, modx, norms, onorms, w_br_hg, w_br_gla, w_out, w_gate, w_up, w_down, target)


def _mixer_tail_fwd(x_lat, p, o_list, modx, norms, onorms, w_br_hg, w_br_gla, w_out, n_ctx_tiles, name):
    rows, d = x_lat.shape

    def body(x_ref, ofw_hg, obw_hg, ofw_gla, obw_gla, p_hgate, p_ggate, p_ghg_a, p_ghg_b, p_ggla_a, p_ggla_b,
             modx_ref, norm_ref, on_ref, wbh_ref, wbg_ref, wout_ref, z2_ref, y1_ref, mrg_ref, oghg_ref, oggla_ref):
        p_gate_hg = jnp.concatenate([p_ghg_a[...], p_ghg_b[...]], axis=1)
        p_gate_gla = jnp.concatenate([p_ggla_a[...], p_ggla_b[...]], axis=1)
        t = _mixer_tail(x_ref[...], ofw_hg[...] + obw_hg[...], ofw_gla[...] + obw_gla[...], p_hgate[...],
                        p_ggate[...], p_gate_hg, p_gate_gla, on_ref[0:1, 0:HD], on_ref[1:2, 0:HD],
                        wbh_ref[...], wbg_ref[...], wout_ref[...])
        y1_ref[...] = t["y1"]
        mrg_ref[...] = t["merged"]
        oghg_ref[...] = t["og_hg"]
        oggla_ref[...] = t["og_gla"]
        n1, _ = _rms(t["y1"])
        z2_ref[...] = x_ref[...] + n1 * norm_ref[1:2, :] * modx_ref[2:3, :]

    lat = lambda i: (i, 0)
    full = lambda i: (i + n_ctx_tiles, 0)
    fixed = lambda i: (0, 0)

    def pcol(blk):
        return pl.BlockSpec((TM, HW), lambda i: (i + n_ctx_tiles, blk))

    in_specs = ([pl.BlockSpec((TM, d), lat)] + [pl.BlockSpec((TM, HW), full)] * 4
                + [pcol(C_HGATE), pcol(C_GGATE), pcol(9), pcol(10), pcol(11), pcol(12)]
                + [pl.BlockSpec((8, d), fixed)] * 3 + [VMEM_SPEC] * 3)
    bf = lambda w: jax.ShapeDtypeStruct((rows, w), BF16)
    f32 = jax.ShapeDtypeStruct((rows, d), F32)
    return pl.pallas_call(
        body, name=name, grid=(rows // TM,), out_shape=[f32, f32, bf(d), bf(HW), bf(HW)], in_specs=in_specs,
        out_specs=[pl.BlockSpec((TM, d), lat)] * 3 + [pl.BlockSpec((TM, HW), lat)] * 2,
        compiler_params=_cparams(dimension_semantics=("parallel",)),
    )(x_lat, *o_list, p, p, p, p, p, p, modx, norms, onorms, w_br_hg, w_br_gla, w_out)


def _ffn_fwd_bwd(z2, modx, norms, w_gate, w_up, w_down, target, name):
    rows, d = z2.shape
    dff = w_gate.shape[1]
    inv_d = 1.0 / d

    def body(z2_ref, modx_ref, norm_ref, wg_ref, wu_ref, wd_ref, t_ref,
             loss_ref, dz2_ref, h2_ref, a_ref, du_ref, dv_ref, dy2_ref, stat_ref):
        i = pl.program_id(0)
        pre2, post2 = norm_ref[2:3, :], norm_ref[3:4, :]
        shift2, scale2, gate2 = modx_ref[3:4, :], modx_ref[4:5, :], modx_ref[5:6, :]
        z2 = z2_ref[...]
        n2, r2 = _rms(z2)
        nw2 = n2 * pre2
        h2 = (nw2 * (1.0 + scale2) + shift2).astype(BF16)
        h2_ref[...] = h2
        u = jnp.dot(h2, wg_ref[...], preferred_element_type=F32)
        v = jnp.dot(h2, wu_ref[...], preferred_element_type=F32)
        su = _silu(u)
        a = (su * v).astype(BF16)
        a_ref[...] = a
        y2 = jnp.dot(a, wd_ref[...], preferred_element_type=F32)
        n3, r3 = _rms(y2)
        err = z2 + n3 * post2 * gate2 - t_ref[...]
        part = 0.5 * inv_d * jnp.sum(err * err)
        dz3 = err * inv_d
        dgate2 = _colsum(dz3 * n3 * post2)
        tt = dz3 * gate2
        dpost2 = _colsum(tt * n3)
        dy2 = _rms_bwd(tt * post2, n3, r3).astype(BF16)
        dy2_ref[...] = dy2
        da = _dot_nt(dy2, wd_ref[...])
        du = (da * v * _dsilu(u)).astype(BF16)
        dv = (da * su).astype(BF16)
        du_ref[...] = du
        dv_ref[...] = dv
        dh2 = _dot_nt(du, wg_ref[...]) + _dot_nt(dv, wu_ref[...])
        dshift2 = _colsum(dh2)
        dscale2 = _colsum(dh2 * nw2)
        dnw2 = dh2 * (1.0 + scale2)
        dpre2 = _colsum(dnw2 * n2)
        dz2_ref[...] = dz3 + _rms_bwd(dnw2 * pre2, n2, r2)

        @pl.when(i == 0)
        def _():
            stat_ref[...] = jnp.zeros_like(stat_ref)
            loss_ref[...] = jnp.zeros_like(loss_ref)

        for r, val in enumerate((dshift2, dscale2, dgate2, dpre2, dpost2)):
            stat_ref[r:r + 1, :] += val
        loss_ref[...] += part

    lat = lambda i: (i, 0)
    fixed = lambda i: (0, 0)
    bf = lambda w: jax.ShapeDtypeStruct((rows, w), BF16)
    return pl.pallas_call(
        body, name=name, grid=(rows // TM,),
        out_shape=[jax.ShapeDtypeStruct((8, 128), F32), jax.ShapeDtypeStruct((rows, d), F32), bf(d), bf(dff), bf(dff),
                   bf(dff), bf(d), jax.ShapeDtypeStruct((8, d), F32)],
        in_specs=[pl.BlockSpec((TM, d), lat), pl.BlockSpec((8, d), fixed), pl.BlockSpec((8, d), fixed)]
        + [VMEM_SPEC] * 3 + [pl.BlockSpec((TM, d), lat)],
        out_specs=[pl.BlockSpec((8, 128), fixed), pl.BlockSpec((TM, d), lat), pl.BlockSpec((TM, d), lat),
                   pl.BlockSpec((TM, dff), lat), pl.BlockSpec((TM, dff), lat), pl.BlockSpec((TM, dff), lat),
                   pl.BlockSpec((TM, d), lat), pl.BlockSpec((8, d), fixed)],
        compiler_params=_cparams(dimension_semantics=("arbitrary",)),
    )(z2, modx, norms, w_gate, w_up, w_down, target)


def _mixer_tail_bwd(x_lat, p, o_list, dz2, y1, modx, norms, onorms, w_br_hg, w_br_gla, w_out, n_ctx_tiles, n_tiles,
                    name):
    rows, d = x_lat.shape
    total = n_tiles * TM

    def body(x_ref, ofw_hg, obw_hg, ofw_gla, obw_gla, p_hgate, p_ggate, p_ghg_a, p_ghg_b, p_ggla_a, p_ggla_b,
             dz2_ref, y1_ref, modx_ref, norm_ref, on_ref, wbh_ref, wbg_ref, wout_ref,
             dohg_ref, dogla_ref, dhgate_ref, dggate_ref, dghg_ref, dggla_ref, dy1_ref, dbhg_ref, dbgla_ref,
             stat_ref):
        i = pl.program_id(0)

        @pl.when(i == 0)
        def _():
            stat_ref[...] = jnp.zeros_like(stat_ref)

        @pl.when(i < n_ctx_tiles)
        def _():
            for ref in (dohg_ref, dogla_ref, dhgate_ref, dggate_ref, dghg_ref, dggla_ref):
                ref[...] = jnp.zeros_like(ref)

        @pl.when(i >= n_ctx_tiles)
        def _():
            post1, gate1 = norm_ref[1:2, :], modx_ref[2:3, :]
            hg_on, gla_on = on_ref[0:1, 0:HD], on_ref[1:2, 0:HD]
            p_gate_hg = jnp.concatenate([p_ghg_a[...], p_ghg_b[...]], axis=1)
            p_gate_gla = jnp.concatenate([p_ggla_a[...], p_ggla_b[...]], axis=1)
            ph, pg = p_hgate[...], p_ggate[...]
            t = _mixer_tail(x_ref[...], ofw_hg[...] + obw_hg[...], ofw_gla[...] + obw_gla[...], ph, pg,
                            p_gate_hg, p_gate_gla, hg_on, gla_on, wbh_ref[...], wbg_ref[...], wout_ref[...])
            dz2 = dz2_ref[...]
            n1, r1 = _rms(y1_ref[...])
            dgate1 = _colsum(dz2 * n1 * post1)
            tt = dz2 * gate1
            dpost1 = _colsum(tt * n1)
            dy1 = _rms_bwd(tt * post1, n1, r1).astype(BF16)
            dy1_ref[...] = dy1
            dmerged = _dot_nt(dy1, wout_ref[...])
            dghg_ref[...] = dmerged * t["b_hg"] * t["s_hg"] * (1.0 - t["s_hg"])
            dggla_ref[...] = dmerged * t["b_gla"] * t["s_gla"] * (1.0 - t["s_gla"])
            db_hg = (dmerged * t["s_hg"]).astype(BF16)
            db_gla = (dmerged * t["s_gla"]).astype(BF16)
            dbhg_ref[...] = db_hg
            dbgla_ref[...] = db_gla
            don_acc = []
            for (db, wb, pgate, on, ns, rs, gain, gate_ref, do_ref) in (
                    (db_hg, wbh_ref, ph, t["on_hg"], t["n_hg"], t["r_hg"], hg_on, dhgate_ref, dohg_ref),
                    (db_gla, wbg_ref, pg, t["on_gla"], t["n_gla"], t["r_gla"], gla_on, dggate_ref, dogla_ref)):
                dog = _dot_nt(db, wb[...])
                gate_ref[...] = dog * on * _dsilu(pgate)
                don = dog * _silu(pgate)
                acc = jnp.zeros((1, HD), F32)
                for h in range(NH):
                    sl = slice(h * HD, (h + 1) * HD)
                    acc = acc + _colsum(don[:, sl] * ns[h])
                    do_ref[:, sl] = _rms_bwd(don[:, sl] * gain, ns[h], rs[h])
                don_acc.append(acc)
            stat_ref[0:1, :] += dgate1
            stat_ref[1:2, :] += dpost1
            stat_ref[2:3, 0:HD] += don_acc[0]
            stat_ref[2:3, HD:2 * HD] += don_acc[1]

    lat = lambda i: (jnp.maximum(i - n_ctx_tiles, 0), 0)
    full = lambda i: (i, 0)
    fixed = lambda i: (0, 0)

    def pcol(blk):
        return pl.BlockSpec((TM, HW), lambda i: (i, blk))

    in_specs = ([pl.BlockSpec((TM, d), lat)] + [pl.BlockSpec((TM, HW), full)] * 4
                + [pcol(C_HGATE), pcol(C_GGATE), pcol(9), pcol(10), pcol(11), pcol(12)]
                + [pl.BlockSpec((TM, d), lat), pl.BlockSpec((TM, d), lat)]
                + [pl.BlockSpec((8, d), fixed)] * 3 + [VMEM_SPEC] * 3)
    f = lambda w: jax.ShapeDtypeStruct((total, w), F32)
    out_shape = [f(HW), f(HW), f(HW), f(HW), f(d), f(d), jax.ShapeDtypeStruct((rows, d), BF16),
                 jax.ShapeDtypeStruct((rows, d), BF16), jax.ShapeDtypeStruct((rows, d), BF16),
                 jax.ShapeDtypeStruct((8, d), F32)]
    out_specs = ([pl.BlockSpec((TM, HW), full)] * 4 + [pl.BlockSpec((TM, d), full)] * 2
                 + [pl.BlockSpec((TM, d), lat)] * 3 + [pl.BlockSpec((8, d), fixed)])
    return pl.pallas_call(
        body, name=name, grid=(n_tiles,), out_shape=out_shape, in_specs=in_specs, out_specs=out_specs,
        compiler_params=_cparams(dimension_semantics=("arbitrary",)),
    )(x_lat, *o_list, p, p, p, p, p, p, dz2, y1, modx, norms, onorms, w_br_hg, w_br_gla, w_out)


def _in_projection_bwd(z, dz2, modc, modx, pre1, w_r, pieces, n_ctx_tiles, name):
    rows, d = z.shape
    lat_rows = dz2.shape[0]
    width = w_r.shape[1]
    n_pieces = len(pieces)

    def body(*refs):
        z_ref, dz2_ref, modc_ref, modx_ref, pre_ref, w_ref = refs[:6]
        (dhq_f, dhq_b, dhi_f, dhi_b, dhf_f, dhf_b, dhgate, dgq_f, dgq_b, dgk_f, dgk_b, dgv_f, dgv_b, dggate,
         dghg, dggla, dlr_f, dlr_b) = refs[6:6 + n_pieces]
        dp_ref, gx_ref, stat_ref = refs[6 + n_pieces:]
        i = pl.program_id(0)
        is_ctx = i < n_ctx_tiles
        sections = [
            (0, dhq_f[...] + dhq_b[...]), (HW, dhi_f[...] + dhi_b[...]), (2 * HW, dhf_f[...]), (3 * HW, dhf_b[...]),
            (4 * HW, dhgate[...]), (5 * HW, dgq_f[...] + dgq_b[...]), (6 * HW, dgk_f[...] + dgk_b[...]),
            (7 * HW, dgv_f[...] + dgv_b[...]), (8 * HW, dggate[...]),
            (9 * HW, dghg[:, 0:HW]), (10 * HW, dghg[:, HW:2 * HW]),
            (11 * HW, dggla[:, 0:HW]), (12 * HW, dggla[:, HW:2 * HW]), (OFF_LR, dlr_f[...] + dlr_b[...])]
        dh = jnp.zeros((TM, d), F32)
        for off, val in sections:
            w = val.shape[1]
            vb = val.astype(BF16)
            dp_ref[:, off:off + w] = vb
            dh = dh + _dot_nt(vb, w_ref[:, off:off + w])
        n, r = _rms(z_ref[...])
        pre = pre_ref[...]
        scale = jnp.where(is_ctx, modc_ref[1:2, :], modx_ref[1:2, :])
        nw = n * pre
        dshift = _colsum(dh)
        dscale = _colsum(dh * nw)
        dnw = dh * (1.0 + scale)
        dpre = _colsum(dnw * n)
        gx_ref[...] = dz2_ref[...] + _rms_bwd(dnw * pre, n, r)
        zero = jnp.zeros((1, d), F32)

        @pl.when(i == 0)
        def _():
            stat_ref[...] = jnp.zeros_like(stat_ref)

        stat_ref[0:1, :] += jnp.where(is_ctx, zero, dshift)
        stat_ref[1:2, :] += jnp.where(is_ctx, zero, dscale)
        stat_ref[2:3, :] += jnp.where(is_ctx, dshift, zero)
        stat_ref[3:4, :] += jnp.where(is_ctx, dscale, zero)
        stat_ref[4:5, :] += dpre

    full = lambda i: (i, 0)
    lat = lambda i: (jnp.maximum(i - n_ctx_tiles, 0), 0)
    fixed = lambda i: (0, 0)
    piece_specs = [pl.BlockSpec((TM, a.shape[1]), full) for a in pieces]
    in_specs = [pl.BlockSpec((TM, d), full), pl.BlockSpec((TM, d), lat), pl.BlockSpec((8, d), fixed),
                pl.BlockSpec((8, d), fixed), pl.BlockSpec((1, d), fixed), VMEM_SPEC] + piece_specs
    return pl.pallas_call(
        body, name=name, grid=(rows // TM,),
        out_shape=[jax.ShapeDtypeStruct((rows, width), BF16), jax.ShapeDtypeStruct((lat_rows, d), F32),
                   jax.ShapeDtypeStruct((8, d), F32)],
        in_specs=in_specs,
        out_specs=[pl.BlockSpec((TM, width), full), pl.BlockSpec((TM, d), lat), pl.BlockSpec((8, d), fixed)],
        compiler_params=_cparams(dimension_semantics=("arbitrary",)),
    )(z, dz2, modc, modx, pre1, w_r, *pieces)


def _transposed_lhs_matmul(x_ref, dy_ref, o_ref, xt_ref):
    @pl.when(pl.program_id(1) == 0)
    def _():
        xt_ref[...] = x_ref[...].T

    o_ref[...] = jnp.dot(xt_ref[...], dy_ref[...], preferred_element_type=F32)


def _weight_grad(xs, dy, name, col_block=None, tk=256, tn=512):
    rows = dy.shape[0]
    k = xs.shape[1]
    if col_block is None:
        n, tn_, cb = dy.shape[1], tn, 0
    else:
        n, tn_, cb = col_block[0], col_block[0], col_block[1]
    tn_ = min(tn_, n)
    tk_ = min(tk, k)

    return pl.pallas_call(
        functools.partial(_transposed_lhs_matmul), name=name, grid=(k // tk_, n // tn_),
        out_shape=jax.ShapeDtypeStruct((k, n), F32),
        in_specs=[pl.BlockSpec((rows, tk_), lambda i, j: (0, i)),
                  pl.BlockSpec((rows, tn_), lambda i, j: (0, j + cb))],
        out_specs=pl.BlockSpec((tk_, tn_), lambda i, j: (i, j)),
        scratch_shapes=[pltpu.VMEM((tk_, rows), BF16)],
        compiler_params=_cparams(dimension_semantics=("parallel", "arbitrary")),
    )(xs, dy)


def _chunk_terms(q, k, g, fw):
    c = CHUNK
    r = lax.broadcasted_iota(jnp.int32, (c, c), 0)
    s = lax.broadcasted_iota(jnp.int32, (c, c), 1)
    causal = (s <= r) if fw else (s >= r)
    causal_t = (s >= r) if fw else (s <= r)
    cum = jnp.dot(causal.astype(F32), g, precision=HIGHEST, preferred_element_type=F32)
    row = lax.broadcasted_iota(jnp.int32, (c, 1), 0)
    pos = row if fw else (c - 1 - row)
    starts = [None]
    for j in range(1, NSUB):
        rj = SUB * j - 1 if fw else c - SUB * j
        starts.append(cum[rj:rj + 1, :])
    in_blk = [(pos >= SUB * j) & (pos < SUB * (j + 1)) for j in range(NSUB)]
    e = [jnp.exp(cum)]
    for j in range(1, NSUB):
        e.append(jnp.exp(jnp.where(pos >= SUB * j, cum - starts[j], -1e30)))
    own = jnp.zeros_like(cum)
    for j in range(1, NSUB):
        own = own + jnp.where(in_blk[j], starts[j], 0.0)
    kscale = jnp.exp(own - cum)
    rend = c - 1 if fw else 0
    cend = cum[rend:rend + 1, :]
    tail = jnp.exp(cend - cum)
    qcat = jnp.concatenate([q * e[j] for j in range(NSUB)], axis=1).astype(BF16)
    kt = k * kscale
    km = jnp.concatenate([jnp.where(in_blk[j], kt, 0.0) for j in range(NSUB)], axis=1).astype(BF16)
    return dict(causal=causal, causal_t=causal_t, e=e, in_blk=in_blk, kscale=kscale, cend=cend, tail=tail,
                qcat=qcat, km=km)


def _chunk_fwd(q, k, v, g, st0, fw):
    t = _chunk_terms(q, k, g, fw)
    a = jnp.where(t["causal"], _dot_nt(t["qcat"], t["km"]), 0.0)
    o = _dot(a, v) + _dot_nt(t["qcat"][:, 0:HD], st0)
    st1 = st0 * jnp.exp(t["cend"]) + _dot_tn(v, k * t["tail"])
    return o, st1


def _chunk_bwd(q, k, v, g, st0, do, dst1, fw):
    t = _chunk_terms(q, k, g, fw)
    qcat, km, e = t["qcat"], t["km"], t["e"]
    a_t = jnp.where(t["causal_t"], _dot_nt(km, qcat), 0.0)
    ktail = k * t["tail"]
    dv = _dot(a_t, do) + _dot_nt(ktail, dst1)
    da = jnp.where(t["causal"], _dot_nt(do, v), 0.0)
    da_t = jnp.where(t["causal_t"], _dot_nt(v, do), 0.0)
    dqcat = _dot(da, km)
    dq_inter = e[0] * _dot(do, st0)
    dq = dq_inter
    for j in range(NSUB):
        dq = dq + e[j] * dqcat[:, j * HD:(j + 1) * HD]
    dkm = _dot(da_t, qcat)
    dkt = jnp.zeros_like(k)
    for j in range(NSUB):
        dkt = dkt + jnp.where(t["in_blk"][j], dkm[:, j * HD:(j + 1) * HD], 0.0)
    dk_inter = _dot(v, dst1) * t["tail"]
    dk = dkt * t["kscale"] + dk_inter
    dcum = q * dq_inter - k * dk_inter
    for j in range(NSUB):
        sl = slice(j * HD, (j + 1) * HD)
        dcum = dcum + qcat[:, sl].astype(F32) * dqcat[:, sl] - km[:, sl].astype(F32) * dkm[:, sl]
    ecend = jnp.exp(t["cend"])
    end = ecend * _colsum(st0 * dst1) + _colsum(k * dk_inter)
    dg = jnp.dot(t["causal_t"].astype(F32), dcum, precision=HIGHEST, preferred_element_type=F32) + end
    dst0 = dst1 * ecend + _dot_tn(do, q * e[0])
    return dq, dk, dv, dg, dst0


def _chunk_index(step, n_ctx_chunks, n_chunks, fw):
    if fw:
        return step
    return jnp.where(step < n_ctx_chunks, n_ctx_chunks - 1 - step, n_chunks - 1 + n_ctx_chunks - step)


def _hg_inputs(hq, hf, lbv, d_idx, sl):
    lb = _sigmoid(lbv[d_idx:d_idx + 1, sl] - lbv[2 + d_idx:3 + d_idx, sl])
    sg = _sigmoid(hf)
    f = lb + (1.0 - lb) * sg
    return _silu(hq), 1.0 - f, jnp.log(f), f, sg, lb


def _scan_fwd(p, side, n_ctx_chunks, fw, branch, name):
    rows = p.shape[0]
    n_chunks = rows // CHUNK
    d_idx = 0 if fw else 1
    hg = branch == "hg"
    cols = (C_HQ, C_HI, C_HF_FW + d_idx) if hg else (C_GQ, C_GK, C_GV)

    def body(*refs):
        if hg:
            a_ref, b_ref, c_ref, lb_ref, o_ref, st_ref, state = refs
        else:
            a_ref, b_ref, c_ref, lr_ref, wgk_ref, bgk_ref, o_ref, st_ref, state = refs
            logits = _dot(lr_ref[...], wgk_ref[...]) + bgk_ref[...]
            g_all = _log_sigmoid(logits) * (1.0 / GATE_NORM)

        @pl.when(pl.program_id(0) == 0)
        def _():
            state[...] = jnp.zeros_like(state)

        for h in range(NH):
            sl = slice(h * HD, (h + 1) * HD)
            if hg:
                q, k, g, _, _, _ = _hg_inputs(a_ref[:, sl], c_ref[:, sl], lb_ref[...], d_idx, sl)
                v = b_ref[:, sl]
            else:
                q, k, v, g = a_ref[:, sl] * (HD ** -0.5), b_ref[:, sl], c_ref[:, sl], g_all[:, sl]
            st0 = state[h]
            st_ref[0, h] = st0
            o, st1 = _chunk_fwd(q, k, v, g, st0, fw)
            o_ref[:, sl] = o
            state[h] = st1

    def cmap(blk):
        return pl.BlockSpec((CHUNK, HW), lambda j: (_chunk_index(j, n_ctx_chunks, n_chunks, fw), blk))

    fixed = lambda j: (0, 0)
    in_specs = [cmap(cols[0]), cmap(cols[1]), cmap(cols[2])]
    if hg:
        in_specs += [pl.BlockSpec((4, HW), fixed)]
        args = (p, p, p, side)
    else:
        in_specs += [pl.BlockSpec((CHUNK, 128), lambda j: (_chunk_index(j, n_ctx_chunks, n_chunks, fw), OFF_LR // 128)),
                     pl.BlockSpec((128, HW), fixed), pl.BlockSpec((1, HW), fixed)]
        args = (p, p, p, p, side[0], side[1])
    return pl.pallas_call(
        body, name=name, grid=(n_chunks,),
        out_shape=[jax.ShapeDtypeStruct((rows, HW), F32), jax.ShapeDtypeStruct((n_chunks, NH, HD, HD), F32)],
        in_specs=in_specs,
        out_specs=[pl.BlockSpec((CHUNK, HW), lambda j: (_chunk_index(j, n_ctx_chunks, n_chunks, fw), 0)),
                   pl.BlockSpec((1, NH, HD, HD), lambda j: (_chunk_index(j, n_ctx_chunks, n_chunks, fw), 0, 0, 0))],
        scratch_shapes=[pltpu.VMEM((NH, HD, HD), F32)],
        compiler_params=_cparams(dimension_semantics=("arbitrary",)),
    )(*args)


def _scan_bwd(p, side, states, d_o, n_ctx_chunks, fw, branch, name):
    rows = p.shape[0]
    n_chunks = rows // CHUNK
    d_idx = 0 if fw else 1
    hg = branch == "hg"
    cols = (C_HQ, C_HI, C_HF_FW + d_idx) if hg else (C_GQ, C_GK, C_GV)

    def body(*refs):
        if hg:
            a_ref, b_ref, c_ref, lb_ref, st_ref, do_ref, da_ref, db_ref, dc_ref, dlb_ref, dstate = refs
        else:
            (a_ref, b_ref, c_ref, lr_ref, wgk_ref, bgk_ref, st_ref, do_ref, da_ref, db_ref, dc_ref, dlr_ref,
             dwgk_ref, dbias_ref, dstate) = refs
            lr = lr_ref[...]
            logits = _dot(lr, wgk_ref[...]) + bgk_ref[...]
            g_all = _log_sigmoid(logits) * (1.0 / GATE_NORM)

        @pl.when(pl.program_id(0) == 0)
        def _():
            dstate[...] = jnp.zeros_like(dstate)
            if hg:
                dlb_ref[...] = jnp.zeros_like(dlb_ref)
            else:
                dwgk_ref[...] = jnp.zeros_like(dwgk_ref)
                dbias_ref[...] = jnp.zeros_like(dbias_ref)

        dg_parts = []
        for h in range(NH):
            sl = slice(h * HD, (h + 1) * HD)
            if hg:
                hq, hf = a_ref[:, sl], c_ref[:, sl]
                q, k, g, f, sg, lb = _hg_inputs(hq, hf, lb_ref[...], d_idx, sl)
                v = b_ref[:, sl]
            else:
                q, k, v, g = a_ref[:, sl] * (HD ** -0.5), b_ref[:, sl], c_ref[:, sl], g_all[:, sl]
            dq, dk, dv, dg, dst0 = _chunk_bwd(q, k, v, g, st_ref[0, h], do_ref[:, sl], dstate[h], fw)
            dstate[h] = dst0
            if hg:
                da_ref[:, sl] = dq * _dsilu(hq)
                db_ref[:, sl] = dv
                df = dg / f - dk
                dc_ref[:, sl] = df * (1.0 - lb) * sg * (1.0 - sg)
                dlb_ref[0:1, sl] += _colsum(df * (1.0 - sg))
            else:
                da_ref[:, sl] = dq * (HD ** -0.5)
                db_ref[:, sl] = dk
                dc_ref[:, sl] = dv
                dg_parts.append(dg)
        if not hg:
            dlogits = jnp.concatenate(dg_parts, axis=1) * (1.0 / GATE_NORM) * (1.0 - _sigmoid(logits))
            dlr_ref[...] = _dot_nt(dlogits, wgk_ref[...])
            dwgk_ref[...] += _dot_tn(lr, dlogits)
            dbias_ref[0:1, :] += _colsum(dlogits)

    def chunk_of(j):
        return _chunk_index(n_chunks - 1 - j, n_ctx_chunks, n_chunks, fw)

    def cmap(blk, width=HW):
        return pl.BlockSpec((CHUNK, width), lambda j: (chunk_of(j), blk))

    fixed = lambda j: (0, 0)
    st_spec = pl.BlockSpec((1, NH, HD, HD), lambda j: (chunk_of(j), 0, 0, 0))
    big = jax.ShapeDtypeStruct((rows, HW), F32)
    if hg:
        in_specs = [cmap(cols[0]), cmap(cols[1]), cmap(cols[2]), pl.BlockSpec((4, HW), fixed), st_spec, cmap(0)]
        args = (p, p, p, side, states, d_o)
        out_shape = [big, big, big, jax.ShapeDtypeStruct((8, HW), F32)]
        out_specs = [cmap(0), cmap(0), cmap(0), pl.BlockSpec((8, HW), fixed)]
    else:
        in_specs = [cmap(cols[0]), cmap(cols[1]), cmap(cols[2]), cmap(OFF_LR // 128, 128),
                    pl.BlockSpec((128, HW), fixed), pl.BlockSpec((1, HW), fixed), st_spec, cmap(0)]
        args = (p, p, p, p, side[0], side[1], states, d_o)
        out_shape = [big, big, big, jax.ShapeDtypeStruct((rows, 128), F32), jax.ShapeDtypeStruct((128, HW), F32),
                     jax.ShapeDtypeStruct((8, HW), F32)]
        out_specs = [cmap(0), cmap(0), cmap(0), cmap(0, 128), pl.BlockSpec((128, HW), fixed),
                     pl.BlockSpec((8, HW), fixed)]
    return pl.pallas_call(
        body, name=name, grid=(n_chunks,), out_shape=out_shape, in_specs=in_specs, out_specs=out_specs,
        scratch_shapes=[pltpu.VMEM((NH, HD, HD), F32)],
        compiler_params=_cparams(dimension_semantics=("arbitrary",)),
    )(*args)


SMALL_ROWS = 56
ROWS_MOD_X = (0, 1, 8, 16, 17, 18)
ROWS_MOD_C = (2, 3)
ROW_PRE1, ROW_POST1, ROW_ONORM, ROW_PRE2, ROW_POST2, ROW_LB, ROW_BGK, ROW_WGK = 4, 9, 10, 19, 20, 24, 32, 40


def _reduce_small(gathered, lb_full, name):
    _, _, d = gathered.shape

    def body(g_ref, lb_ref, sum_ref, dmod_ref, dbmod_ref, dlb_ref):
        total = g_ref[0]
        for b in range(1, N_DEV):
            total = total + g_ref[b]
        sum_ref[...] = total
        dmod_ref[...] = jnp.zeros_like(dmod_ref)
        for m in range(N_MOD):
            col = slice(m * d, (m + 1) * d)
            acc = jnp.zeros((1, d), F32)
            for b in range(N_DEV):
                row = g_ref[b, ROWS_MOD_X[m]:ROWS_MOD_X[m] + 1, :]
                dmod_ref[b:b + 1, col] = row
                acc = acc + row
            if m < 2:
                ctx_row = total[ROWS_MOD_C[m]:ROWS_MOD_C[m] + 1, :]
                dmod_ref[8:9, col] = ctx_row
                acc = acc + ctx_row
            dbmod_ref[:, col] = acc
        lbv = lb_ref[...]
        for dd in range(2):
            lb = _sigmoid(lbv[dd:dd + 1, :] - lbv[2 + dd:3 + dd, :])
            gl = total[ROW_LB:ROW_LB + 1, dd * HW:(dd + 1) * HW] * lb * (1.0 - lb)
            dlb_ref[dd:dd + 1, :] = gl
            dlb_ref[2 + dd:3 + dd, :] = -gl

    return pl.pallas_call(
        body, name=name,
        out_shape=[jax.ShapeDtypeStruct((SMALL_ROWS, d), F32), jax.ShapeDtypeStruct((16, N_MOD * d), F32),
                   jax.ShapeDtypeStruct((1, N_MOD * d), F32), jax.ShapeDtypeStruct((4, HW), F32)],
        in_specs=[VMEM_SPEC] * 2, out_specs=[VMEM_SPEC] * 4, compiler_params=_cparams(),
    )(gathered, lb_full)


def _c_ctx_grad(gathered, c_ctx_row, name):
    def body(g_ref, c_ref, o_ref):
        acc = g_ref[0, 0:1, :]
        for chip in range(1, N_CHIP):
            acc = acc + g_ref[2 * chip, 0:1, :]
        o_ref[...] = acc * _dsilu(c_ref[...])

    return pl.pallas_call(
        body, name=name, out_shape=jax.ShapeDtypeStruct(c_ctx_row.shape, F32),
        in_specs=[VMEM_SPEC] * 2, out_specs=VMEM_SPEC, compiler_params=_cparams(),
    )(gathered, c_ctx_row)


def _relayout_w_in(w):
    pad = jnp.zeros((w.shape[0], 128 - 2 * RANK), w.dtype)
    return jnp.concatenate([w[:, :9 * HW], w[:, 9 * HW + 2 * RANK:], w[:, 9 * HW:9 * HW + 2 * RANK], pad], axis=1)


def _w_in_grad_blocks(g_main, g_lr, n_blocks):
    lr0 = 9 * HW
    n = (g_main.shape[1] + 2 * RANK) // n_blocks

    def cols(lo, hi):
        out = []
        if lo < lr0:
            out.append(g_main[:, lo:min(hi, lr0)])
        if hi > lr0 and lo < lr0 + 2 * RANK:
            out.append(g_lr[:, max(lo, lr0) - lr0:min(hi, lr0 + 2 * RANK) - lr0])
        if hi > lr0 + 2 * RANK:
            out.append(g_main[:, max(lo, lr0 + 2 * RANK) - 2 * RANK:hi - 2 * RANK])
        return out

    return jnp.stack([jnp.concatenate(cols(j * n, (j + 1) * n), axis=1) for j in range(n_blocks)])


def _blocked(full, n_blocks):
    k, n = full.shape
    return full.reshape(k, n_blocks, n // n_blocks).transpose(1, 0, 2)


def _unblocked(blocks):
    nb, k, n = blocks.shape
    return blocks.transpose(1, 0, 2).reshape(k, nb * n)


def _sample_front(x0, ctx0, modc, modx, norm_pre1, lb_full, gla_side, w_in_r):
    ctx_len = ctx0.shape[0]
    n_ctx_tiles = ctx_len // TM
    n_ctx_chunks = ctx_len // CHUNK
    z = jnp.concatenate([ctx0, x0], axis=0)
    h1, p = _in_projection(z, modc, modx, norm_pre1, w_in_r, n_ctx_tiles, "in_projection")
    o_hg_fw, st_hg_fw = _scan_fwd(p, lb_full, n_ctx_chunks, True, "hg", "scan_hg_fw")
    o_hg_bw, st_hg_bw = _scan_fwd(p, lb_full, n_ctx_chunks, False, "hg", "scan_hg_bw")
    o_gla_fw, st_gla_fw = _scan_fwd(p, gla_side[0], n_ctx_chunks, True, "gla", "scan_gla_fw")
    o_gla_bw, st_gla_bw = _scan_fwd(p, gla_side[1], n_ctx_chunks, False, "gla", "scan_gla_bw")
    return dict(z=z, h1=h1, p=p, o_list=[o_hg_fw, o_hg_bw, o_gla_fw, o_gla_bw],
                states=[st_hg_fw, st_hg_bw, st_gla_fw, st_gla_bw])


def _sample_back(reduce, front, x0, ctx0, target0, modc, modx, norm_pre1, norms, onorms, lb_full, gla_side, w_in_r,
                 wbh, wbg, wout, wg, wu, wd):
    seq, d = x0.shape
    ctx_len = ctx0.shape[0]
    n_ctx_tiles = ctx_len // TM
    n_tiles = (ctx_len + seq) // TM
    n_ctx_chunks = ctx_len // CHUNK
    z, h1, p, o_list = front["z"], front["h1"], front["p"], front["o_list"]
    st_hg_fw, st_hg_bw, st_gla_fw, st_gla_bw = front["states"]
    z2, y1, merged, og_hg, og_gla = _mixer_tail_fwd(x0, p, o_list, modx, norms, onorms, wbh, wbg, wout, n_ctx_tiles,
                                                    "mixer_tail")
    loss_part, dz2, h2, a_act, du, dv, dy2, stat_ffn = _ffn_fwd_bwd(z2, modx, norms, wg, wu, wd, target0, "ffn")
    dff = wg.shape[1]
    tn_ff = dff // N_CHIP if (dff // N_CHIP) % 128 == 0 else 256
    tok = reduce("ffn", [_weight_grad(h2, du, "grad_w_ff_gate", tn=tn_ff),
                         _weight_grad(h2, dv, "grad_w_ff_up", tn=tn_ff),
                         _weight_grad(a_act, dy2, "grad_w_ff_down", tk=tn_ff)])

    (d_ohg, d_ogla, d_hgate, d_ggate, d_ghg, d_ggla, dy1, db_hg, db_gla, stat_mix) = _mixer_tail_bwd(
        x0, p, o_list, dz2, y1, modx + tok, norms, onorms, wbh, wbg, wout, n_ctx_tiles, n_tiles, "mixer_tail_bwd")
    tok = reduce("mix", [_weight_grad(og_hg, db_hg, "grad_w_br_hg"), _weight_grad(og_gla, db_gla, "grad_w_br_gla"),
                         _weight_grad(merged, dy1, "grad_w_out")])
    lb_b = lb_full + tok
    gla_b = [(wgk, bias + tok) for wgk, bias in gla_side]
    dhq_f, dhi_f, dhf_f, dlb_f = _scan_bwd(p, lb_b, st_hg_fw, d_ohg, n_ctx_chunks, True, "hg", "scan_hg_fw_bwd")
    dhq_b, dhi_b, dhf_b, dlb_b = _scan_bwd(p, lb_b, st_hg_bw, d_ohg, n_ctx_chunks, False, "hg", "scan_hg_bw_bwd")
    dgq_f, dgk_f, dgv_f, dlr_f, dwgk_f, dbgk_f = _scan_bwd(p, gla_b[0], st_gla_fw, d_ogla, n_ctx_chunks, True, "gla",
                                                           "scan_gla_fw_bwd")
    dgq_b, dgk_b, dgv_b, dlr_b, dwgk_b, dbgk_b = _scan_bwd(p, gla_b[1], st_gla_bw, d_ogla, n_ctx_chunks, False, "gla",
                                                           "scan_gla_bw_bwd")
    pieces = [dhq_f, dhq_b, dhi_f, dhi_b, dhf_f, dhf_b, d_hgate, dgq_f, dgq_b, dgk_f, dgk_b, dgv_f, dgv_b, d_ggate,
              d_ghg, d_ggla, dlr_f, dlr_b]
    dp, grad_x, stat_in = _in_projection_bwd(z, dz2, modc, modx, norm_pre1, w_in_r, pieces, n_ctx_tiles,
                                             "in_projection_bwd")

    reduce("small", dict(stat_in=stat_in, stat_mix=stat_mix, stat_ffn=stat_ffn, dlb=(dlb_f, dlb_b),
                         dwgk=(dwgk_f, dwgk_b), dbgk=(dbgk_f, dbgk_b)))
    g_in_main = _weight_grad_cols(h1, dp, OFF_LR, "grad_w_in_main")
    g_in_lr = _weight_grad(h1, dp, "grad_w_in_lr", col_block=(128, OFF_LR // 128))
    reduce("in", [_w_in_grad_blocks(g_in_main, g_in_lr, N_CHIP)])
    return dict(
        loss_part=loss_part, grad_x=grad_x, stat_in=stat_in, stat_mix=stat_mix, stat_ffn=stat_ffn,
        dlb=(dlb_f, dlb_b), dwgk=(dwgk_f, dwgk_b), dbgk=(dbgk_f, dbgk_b))


def kernel(x, c, ctx, c_ctx, w_mod, b_mod, norm_pre1, norm_post1, norm_pre2, norm_post2, w_in, hg_lb, hg_onorm, gla_w_gk, gla_b_gk, gla_onorm, w_br_hg, w_br_gla, w_out, w_ff_gate, w_ff_up, w_ff_down, loss_target, m_c_ctx, m_w_mod, m_b_mod, m_norm_pre1, m_norm_post1, m_norm_pre2, m_norm_post2, m_w_in, m_hg_lb, m_hg_onorm, m_gla_w_gk, m_gla_b_gk, m_gla_onorm, m_w_br_hg, m_w_br_gla, m_w_out, m_w_ff_gate, m_w_ff_up, m_w_ff_down, v_c_ctx, v_w_mod, v_b_mod, v_norm_pre1, v_norm_post1, v_norm_pre2, v_norm_post2, v_w_in, v_hg_lb, v_hg_onorm, v_gla_w_gk, v_gla_b_gk, v_gla_onorm, v_w_br_hg, v_w_br_gla, v_w_out, v_w_ff_gate, v_w_ff_up, v_w_ff_down):
    seq, d = x.shape[1], x.shape[2]
    ctx_len = ctx.shape[1]
    assert seq % TM == 0 and ctx_len % TM == 0 and d == 2 * HW
    ax, ay, ac = lax.axis_index("x"), lax.axis_index("y"), lax.axis_index("c")
    chip = 2 * ax + ay
    dev = 2 * chip + ac
    c_arr = jnp.reshape(ac, (1,)).astype(jnp.int32)

    nc = d // 128
    pad8 = lambda a: jnp.pad(a, ((0, -a.shape[0] % 8), (0, 0)))
    small1 = jnp.concatenate([c.reshape(nc, 128), pad8(hg_lb.reshape(4, 128)), gla_w_gk.reshape(2 * RANK, 128),
                              pad8(gla_b_gk.reshape(2, 128))], axis=0)
    got1 = _allgather8(small1, "gather_small_params")
    c_all = got1[:, :nc, :].reshape(N_DEV, d)
    per_chip = got1[0::2]
    lb_full = per_chip[:, nc:nc + 4, :].transpose(1, 0, 2).reshape(4, HW)
    wgk_full = per_chip[:, nc + 8:nc + 8 + 2 * RANK, :].transpose(1, 0, 2).reshape(2, RANK, HW)
    bgk_full = per_chip[:, nc + 8 + 2 * RANK:nc + 10 + 2 * RANK, :].transpose(1, 0, 2).reshape(2, HW)
    wgk_pad = [jnp.zeros((128, HW), F32).at[dd * RANK:(dd + 1) * RANK].set(wgk_full[dd]) for dd in range(2)]
    bgk = [bgk_full[dd:dd + 1] for dd in range(2)]

    n_mod_cols = w_mod.shape[2]
    cond = jnp.concatenate([c_all, pad8(c_ctx.reshape(1, d))], axis=0)
    b_cols = lax.dynamic_slice(b_mod, (0, chip * n_mod_cols), (1, n_mod_cols))
    mod_part = _mod_forward(cond, w_mod[0], b_cols, "mod_forward")
    mod_got = _allgather8(mod_part, "gather_mod")
    mod_all = mod_got[0::2].transpose(1, 0, 2).reshape(16, N_CHIP * n_mod_cols)
    modx = pad8(lax.dynamic_slice(mod_all, (dev, 0), (1, N_MOD * d)).reshape(N_MOD, d))
    modc = pad8(mod_all[8].reshape(N_MOD, d))

    chip_arr = jnp.reshape(chip, (1,)).astype(jnp.int32)
    blocks = [_cast_into_blocks(chip_arr, w_[0], "cast_" + nm) for w_, nm in (
        (w_in, "w_in"), (w_br_hg, "w_br_hg"), (w_br_gla, "w_br_gla"), (w_out, "w_out"), (w_ff_gate, "w_ff_gate"),
        (w_ff_up, "w_ff_up"), (w_ff_down, "w_ff_down"))]
    gathered_in = _gather_blocks(blocks[:1], "gather_w_in")
    sems, lands, token = _blocks_start(blocks[1:], "gather_rest_start")
    w_in_r = _relayout_w_in(_unblocked(gathered_in[0]))

    norms = jnp.concatenate([norm_pre1, norm_post1, norm_pre2, norm_post2, jnp.zeros((4, d), F32)], axis=0)
    onorms = jnp.zeros((8, d), F32).at[0, :HD].set(hg_onorm[0]).at[1, :HD].set(gla_onorm[0])
    gla_side = [(wgk_pad[dd], bgk[dd]) for dd in range(2)]
    modx = modx + token[0, 0]
    front = _sample_front(x[0], ctx[0], modc, modx, norm_pre1, lb_full, gla_side, w_in_r)
    lands = _blocks_wait(sems, lands, front["o_list"][3], "gather_rest_wait")
    gathered = _blocks_finish(lands, "gather_rest_finish")
    wbh, wbg = _unblocked(gathered[0]), _unblocked(gathered[1])
    wout = gathered[2].reshape(d, d)
    wg, wu = _unblocked(gathered[3]), _unblocked(gathered[4])
    wd = gathered[5].reshape(wg.shape[1], d)
    dff = wg.shape[1]
    groups = {"ffn": ["w_ff_gate", "w_ff_up", "w_ff_down"], "mix": ["w_br_hg", "w_br_gla", "w_out"], "in": ["w_in"]}
    row_sharded = {"w_out": d // N_CHIP, "w_ff_down": dff // N_CHIP}
    in_flight = {}

    small = {}

    def reduce_small(stats):
        small2 = jnp.concatenate([
            stats["stat_in"], stats["stat_mix"], stats["stat_ffn"],
            jnp.concatenate(stats["dlb"], axis=1), jnp.concatenate(stats["dbgk"], axis=1),
            jnp.concatenate([stats["dwgk"][0][0:RANK], stats["dwgk"][1][RANK:2 * RANK]], axis=1)], axis=0)
        assert small2.shape[0] == SMALL_ROWS
        got2 = _allgather8(small2, "gather_small_grads")
        total, dmod_all, g_b_mod, g_lb_full = _reduce_small(got2, lb_full, "reduce_small")
        dmod_cols = lax.dynamic_slice(dmod_all, (0, chip * n_mod_cols), (16, n_mod_cols))
        g_w_mod, cctx_part = _mod_backward(cond, w_mod[0], dmod_cols, "mod_backward")
        got3 = _allgather8(cctx_part, "gather_c_ctx_grad")
        g_c_ctx = _c_ctx_grad(got3, c_ctx.reshape(1, d), "c_ctx_grad")
        small.update(total=total, g_b_mod=g_b_mod, g_lb_full=g_lb_full, g_w_mod=g_w_mod, g_c_ctx=g_c_ctx)

    def reduce(group, grads):
        if group == "small":
            return reduce_small(grads)
        nms = groups[group]
        full = [g if g.ndim == 3 else g.reshape(N_CHIP, row_sharded[nm], d) if nm in row_sharded
                else _blocked(g, N_CHIP) for g, nm in zip(grads, nms)]
        from_sibling = _send_other_half(full, "grads_to_sibling_" + group)
        pairs = [_pair_sum(c_arr, f, r_, "pair_sum_" + nm) for f, r_, nm in zip(full, from_sibling, nms)]
        after = [small["g_c_ctx"], small["total"]] if group == "in" else []
        sems_, pairs, lands_, token_ = _scatter_start(pairs, "grads_to_owner_start_" + group, after)
        in_flight[group] = (sems_, pairs, lands_, token_)
        return token_[0, 0]

    r = _sample_back(reduce, front, x[0], ctx[0], loss_target[0], modc, modx, norm_pre1, norms, onorms, lb_full,
                     gla_side, w_in_r, wbh, wbg, wout, wg, wu, wd)
    loss_part, grad_x, stat_in, stat_mix, stat_ffn = (r[k] for k in ("loss_part", "grad_x", "stat_in", "stat_mix",
                                                                     "stat_ffn"))
    (dlb_f, dlb_b), (dwgk_f, dwgk_b), (dbgk_f, dbgk_b) = r["dlb"], r["dwgk"], r["dbgk"]

    weights = dict(w_in=(w_in, m_w_in, v_w_in), w_br_hg=(w_br_hg, m_w_br_hg, v_w_br_hg),
                   w_br_gla=(w_br_gla, m_w_br_gla, v_w_br_gla), w_out=(w_out, m_w_out, v_w_out),
                   w_ff_gate=(w_ff_gate, m_w_ff_gate, v_w_ff_gate), w_ff_up=(w_ff_up, m_w_ff_up, v_w_ff_up),
                   w_ff_down=(w_ff_down, m_w_ff_down, v_w_ff_down))
    names = ["w_in", "w_br_hg", "w_br_gla", "w_out", "w_ff_gate", "w_ff_up", "w_ff_down"]
    big = {}

    def finish(group, after):
        sems_, pairs, lands_, _ = in_flight[group]
        pairs, lands_ = _scatter_wait(sems_, pairs, lands_, after, "grads_to_owner_wait_" + group)
        own_half = [_sum_owner(chip_arr, pr, g, "chip_sum_" + nm) for pr, g, nm in zip(pairs, lands_, groups[group])]
        other_half = _swap_with_sibling(own_half, "halves_to_sibling_" + group)
        for nm, own, oth in zip(groups[group], own_half, other_half):
            w_, m_, v_ = weights[nm]
            res = _adamw_halves(c_arr, own, oth, w_[0], m_[0], v_[0], "adamw_" + nm)
            big[nm] = [r_[None] for r_ in res]
        return big[groups[group][-1]][1]

    token_in = in_flight["in"][3]
    done_ffn = finish("ffn", [token_in])
    done_mix = finish("mix", [done_ffn])

    total, g_b_mod, g_lb_full, g_w_mod, g_c_ctx = (small[k] for k in ("total", "g_b_mod", "g_lb_full", "g_w_mod",
                                                                      "g_c_ctx"))
    g_pre1, g_post1, g_pre2, g_post2 = (total[r_:r_ + 1] for r_ in (ROW_PRE1, ROW_POST1, ROW_PRE2, ROW_POST2))
    g_hg_on, g_gla_on = total[ROW_ONORM:ROW_ONORM + 1, 0:HD], total[ROW_ONORM:ROW_ONORM + 1, HD:2 * HD]
    n_lb = hg_lb.shape[2]
    g_hg_lb = lax.dynamic_slice(g_lb_full, (0, chip * n_lb), (4, n_lb))
    g_bgk = lax.dynamic_slice(total[ROW_BGK:ROW_BGK + 1].reshape(2, HW), (0, chip * n_lb), (2, n_lb))
    g_wgk_full = total[ROW_WGK:ROW_WGK + RANK].reshape(RANK, 2, HW).transpose(1, 0, 2).reshape(2 * RANK, HW)
    g_wgk = lax.dynamic_slice(g_wgk_full, (0, chip * n_lb), (2 * RANK, n_lb))

    small_items = [
        (g_c_ctx, c_ctx.reshape(1, d), m_c_ctx.reshape(1, d), v_c_ctx.reshape(1, d)),
        (g_b_mod, b_mod, m_b_mod, v_b_mod),
        (g_pre1, norm_pre1, m_norm_pre1, v_norm_pre1),
        (g_post1, norm_post1, m_norm_post1, v_norm_post1),
        (g_pre2, norm_pre2, m_norm_pre2, v_norm_pre2),
        (g_post2, norm_post2, m_norm_post2, v_norm_post2),
        (g_hg_lb, hg_lb.reshape(4, n_lb), m_hg_lb.reshape(4, n_lb), v_hg_lb.reshape(4, n_lb)),
        (g_hg_on, hg_onorm, m_hg_onorm, v_hg_onorm),
        (g_wgk, gla_w_gk.reshape(2 * RANK, n_lb), m_gla_w_gk.reshape(2 * RANK, n_lb), v_gla_w_gk.reshape(2 * RANK, n_lb)),
        (g_bgk, gla_b_gk.reshape(2, n_lb), m_gla_b_gk.reshape(2, n_lb), v_gla_b_gk.reshape(2, n_lb)),
        (g_gla_on, gla_onorm, m_gla_onorm, v_gla_onorm),
    ]
    small_res = _adamw_whole(small_items, "adamw_small")
    mod_res = _adamw_tiled(g_w_mod, w_mod[0], m_w_mod[0], v_w_mod[0], "adamw_w_mod")
    finish("in", [done_mix, mod_res[0], small_res[0][0]])

    loss = lax.psum(loss_part[0, 0], ("x", "y", "c"))

    shapes = dict(c_ctx=c_ctx.shape, b_mod=b_mod.shape, norm_pre1=norm_pre1.shape, norm_post1=norm_post1.shape,
                  norm_pre2=norm_pre2.shape, norm_post2=norm_post2.shape, hg_lb=hg_lb.shape, hg_onorm=hg_onorm.shape,
                  gla_w_gk=gla_w_gk.shape, gla_b_gk=gla_b_gk.shape, gla_onorm=gla_onorm.shape)
    small_names = ["c_ctx", "b_mod", "norm_pre1", "norm_post1", "norm_pre2", "norm_post2", "hg_lb", "hg_onorm",
                   "gla_w_gk", "gla_b_gk", "gla_onorm"]
    grads, deltas, new_m, new_v = {}, {}, {}, {}
    for nm, item, res in zip(small_names, small_items, small_res):
        grads[nm] = item[0].reshape(shapes[nm])
        deltas[nm], new_m[nm], new_v[nm] = (r.reshape(shapes[nm]) for r in res)
    grads["w_mod"] = g_w_mod[None]
    deltas["w_mod"], new_m["w_mod"], new_v["w_mod"] = (r[None] for r in mod_res)
    for nm in names:
        grads[nm], deltas[nm], new_m[nm], new_v[nm] = big[nm]
    order = ["c_ctx", "w_mod", "b_mod", "norm_pre1", "norm_post1", "norm_pre2", "norm_post2", "w_in", "hg_lb",
             "hg_onorm", "gla_w_gk", "gla_b_gk", "gla_onorm", "w_br_hg", "w_br_gla", "w_out", "w_ff_gate", "w_ff_up",
             "w_ff_down"]
    return (loss, grad_x[None], *[grads[n] for n in order], *[deltas[n] for n in order],
            *[new_m[n] for n in order], *[new_v[n] for n in order])


def _weight_grad_cols(xs, dy, n_cols, name, tk=256, tn=512):
    rows = dy.shape[0]
    k = xs.shape[1]

    return pl.pallas_call(
        functools.partial(_transposed_lhs_matmul), name=name, grid=(k // tk, n_cols // tn),
        out_shape=jax.ShapeDtypeStruct((k, n_cols), F32),
        in_specs=[pl.BlockSpec((rows, tk), lambda i, j: (0, i)), pl.BlockSpec((rows, tn), lambda i, j: (0, j))],
        out_specs=pl.BlockSpec((tk, tn), lambda i, j: (i, j)),
        scratch_shapes=[pltpu.VMEM((tk, rows), BF16)],
        compiler_params=_cparams(dimension_semantics=("parallel", "arbitrary")),
    )(xs, dy)
```

```python
import functools

import jax
import jax.numpy as jnp
from jax import lax
from jax.experimental import pallas as pl
from jax.experimental.pallas import tpu as pltpu

F32 = jnp.float32
BF16 = jnp.bfloat16
HIGHEST = lax.Precision.HIGHEST
MESH = pl.DeviceIdType.MESH

EPS = 1e-6
CHUNK = 64
SUB = 16
NSUB = CHUNK // SUB
NH = 4
HD = 128
HW = NH * HD
RANK = 16
GATE_NORM = 16.0
N_MOD = 6
TM = 256
TM_FFN = 128
N_DEV = 8
N_CHIP = 4
VMEM_LIMIT = 56 * 1024 * 1024

ADAM_LR = 0.001
ADAM_B1 = 0.9
ADAM_B2 = 0.999
ADAM_EPS = 1e-08
ADAM_WD = 0.01
ADAM_STEP = 10

VMEM_SPEC = pl.BlockSpec(memory_space=pltpu.VMEM)
ANY_SPEC = pl.BlockSpec(memory_space=pl.ANY)
HBM_SPEC = pl.BlockSpec(memory_space=pltpu.HBM)
SEM_SPEC = pl.BlockSpec(memory_space=pltpu.SEMAPHORE)
EFFECT = pltpu.SideEffectType.DATAFLOW_SIDE_EFFECTING


def _cparams(**kw):
    return pltpu.CompilerParams(vmem_limit_bytes=VMEM_LIMIT, **kw)


def _dot(a, b):
    return jnp.dot(a.astype(BF16), b.astype(BF16), preferred_element_type=F32)


def _dot_nt(a, b):
    return lax.dot_general(a.astype(BF16), b.astype(BF16), (((1,), (1,)), ((), ())), preferred_element_type=F32)


def _dot_tn(a, b):
    return lax.dot_general(a.astype(BF16), b.astype(BF16), (((0,), (0,)), ((), ())), preferred_element_type=F32)


def _sigmoid(x):
    return 1.0 / (1.0 + jnp.exp(-x))


def _silu(x):
    return x * _sigmoid(x)


def _dsilu(x):
    s = _sigmoid(x)
    return s * (1.0 + x * (1.0 - s))


def _log_sigmoid(x):
    return jnp.minimum(x, 0.0) - jnp.log(1.0 + jnp.exp(-jnp.abs(x)))


def _colsum(a):
    return jnp.sum(a, axis=0, keepdims=True)


def _rms(a):
    r = lax.rsqrt(jnp.mean(a * a, axis=-1, keepdims=True) + EPS)
    return a * r, r


def _rms_bwd(dn, n, r):
    return r * (dn - n * jnp.mean(dn * n, axis=-1, keepdims=True))


def _place():
    x, y, c = lax.axis_index("x"), lax.axis_index("y"), lax.axis_index("c")
    chips = [(1 - x, y), (x, 1 - y), (1 - x, 1 - y)]
    return x, y, c, chips


def _allgather8(v, name):
    rows, cols = v.shape

    def body(x_ref, out_ref, send_sems, recv_sems, local_sem):
        x, y, c, chips = _place()
        me, sibling = (x, y, c), (x, y, 1 - c)

        def blk(px, py, pc):
            return out_ref.at[4 * px + 2 * py + pc]

        def copy(k, block, to, src=None):
            return pltpu.make_async_remote_copy(
                src_ref=blk(*block) if src is None else src, dst_ref=blk(*block),
                send_sem=send_sems.at[k], recv_sem=recv_sems.at[k], device_id=to, device_id_type=MESH)

        mine = pltpu.make_async_copy(x_ref, blk(*me), local_sem)
        mine.start()
        first = [copy(0, me, sibling, src=x_ref)]
        first += [copy(1 + j, me, (*chip, c), src=x_ref) for j, chip in enumerate(chips)]
        for cp in first:
            cp.start()
        passed = [copy(4 + j, (*chip, c), sibling) for j, chip in enumerate(chips)]
        for j, chip in enumerate(chips):
            copy(1 + j, (*chip, c), me).wait_recv()
            passed[j].start()
        copy(0, sibling, me).wait_recv()
        for j, chip in enumerate(chips):
            copy(4 + j, (*chip, 1 - c), me).wait_recv()
        for cp in first + passed:
            cp.wait_send()
        mine.wait()

    return pl.pallas_call(
        body, name=name,
        out_shape=jax.ShapeDtypeStruct((N_DEV, rows, cols), v.dtype),
        in_specs=[VMEM_SPEC], out_specs=VMEM_SPEC,
        scratch_shapes=[pltpu.SemaphoreType.DMA((7,)), pltpu.SemaphoreType.DMA((7,)), pltpu.SemaphoreType.DMA],
    )(v)


def _cast_into_blocks(chip_arr, w, name):
    rows, cols = w.shape
    tr = _row_tile(rows, 16, 256)

    def body(chip_ref, w_ref, o_ref):
        o_ref[0] = w_ref[...].astype(BF16)

    return pl.pallas_call(
        body, name=name,
        grid_spec=pltpu.PrefetchScalarGridSpec(
            num_scalar_prefetch=1, grid=(rows // tr,),
            in_specs=[pl.BlockSpec((tr, cols), lambda i, chip_ref: (i, 0))],
            out_specs=pl.BlockSpec((1, tr, cols), lambda i, chip_ref: (chip_ref[0], i, 0))),
        out_shape=jax.ShapeDtypeStruct((N_CHIP, rows, cols), BF16),
        compiler_params=_cparams(dimension_semantics=("parallel",)),
    )(chip_arr, w)


def _half_rows(ref, chip_id, pc):
    h = ref.shape[1] // 2
    return ref.at[chip_id, pl.ds(pl.multiple_of(pc * h, 8), h), :]


def _gather_blocks(lands, name):
    n = len(lands)

    def body(*refs):
        outs = refs[n:2 * n]
        send_sems, recv_sems = refs[2 * n:]
        x, y, c, chips = _place()
        me_chip = 2 * x + y
        sibling = (x, y, 1 - c)

        def copy(k, j, chip_id, pc, to):
            return pltpu.make_async_remote_copy(
                src_ref=_half_rows(outs[k], chip_id, pc), dst_ref=_half_rows(outs[k], chip_id, pc),
                send_sem=send_sems.at[k, j], recv_sem=recv_sems.at[k, j], device_id=to, device_id_type=MESH)

        started = []
        for k in range(n):
            for j, chip in enumerate(chips):
                cp = copy(k, j, me_chip, c, (*chip, c))
                cp.start()
                started.append(cp)
        for k in range(n):
            for j, (px, py) in enumerate(chips):
                copy(k, j, 2 * px + py, c, sibling).wait_recv()
                cp = copy(k, 3 + j, 2 * px + py, c, sibling)
                cp.start()
                started.append(cp)
        for k in range(n):
            for j, (px, py) in enumerate(chips):
                copy(k, 3 + j, 2 * px + py, 1 - c, sibling).wait_recv()
        for cp in started:
            cp.wait_send()

    return pl.pallas_call(
        body, name=name,
        out_shape=[jax.ShapeDtypeStruct(l.shape, l.dtype) for l in lands],
        in_specs=[ANY_SPEC] * n, out_specs=[ANY_SPEC] * n,
        input_output_aliases={i: i for i in range(n)},
        scratch_shapes=[pltpu.SemaphoreType.DMA((n, 6)), pltpu.SemaphoreType.DMA((n, 6))],
    )(*lands)


def _hbm(a):
    return pltpu.with_memory_space_constraint(a, pltpu.HBM)


def _blocks_start(lands, name):
    n = len(lands)
    n_sem = 3 * n

    def body(*refs):
        lnd = refs[:n]
        send_sems, recv_sems = refs[n:n + n_sem], refs[n + n_sem:n + 2 * n_sem]
        token = refs[-1]
        x, y, c, chips = _place()
        me_chip = 2 * x + y
        for k in range(n):
            for j, chip in enumerate(chips):
                pltpu.make_async_remote_copy(
                    src_ref=_half_rows(lnd[k], me_chip, c), dst_ref=_half_rows(lnd[k], me_chip, c),
                    send_sem=send_sems[3 * k + j], recv_sem=recv_sems[3 * k + j],
                    device_id=(*chip, c), device_id_type=MESH).start()
        token[...] = jnp.zeros_like(token)

    out = pl.pallas_call(
        body, name=name,
        out_shape=(*[pltpu.SemaphoreType.DMA(())] * (2 * n_sem),
                   *[pltpu.HBM(l.shape, l.dtype) for l in lands],
                   jax.ShapeDtypeStruct((8, 128), F32)),
        in_specs=[HBM_SPEC] * n,
        out_specs=(*[SEM_SPEC] * (2 * n_sem), *[HBM_SPEC] * n, VMEM_SPEC),
        input_output_aliases={i: 2 * n_sem + i for i in range(n)},
        compiler_params=pltpu.CompilerParams(has_side_effects=EFFECT),
    )(*[_hbm(l) for l in lands])
    return list(out[:2 * n_sem]), list(out[2 * n_sem:2 * n_sem + n]), out[-1]


def _blocks_wait(sems, lands, after, name):
    n = len(lands)
    n_sem = 3 * n

    def body(*refs):
        lnd = refs[:n]
        s_sems, r_sems = refs[n:n + n_sem], refs[n + n_sem:n + 2 * n_sem]
        x, y, c, chips = _place()
        me_chip = 2 * x + y
        for k in range(n):
            for j, (px, py) in enumerate(chips):
                cp = pltpu.make_async_remote_copy(
                    src_ref=_half_rows(lnd[k], me_chip, c), dst_ref=_half_rows(lnd[k], 2 * px + py, c),
                    send_sem=s_sems[3 * k + j], recv_sem=r_sems[3 * k + j],
                    device_id=(px, py, c), device_id_type=MESH)
                cp.wait_send()
                cp.wait_recv()

    out = pl.pallas_call(
        body, name=name,
        out_shape=tuple(pltpu.HBM(l.shape, l.dtype) for l in lands),
        in_specs=[HBM_SPEC] * n + [SEM_SPEC] * (2 * n_sem) + [ANY_SPEC],
        out_specs=[HBM_SPEC] * n,
        input_output_aliases={i: i for i in range(n)},
        compiler_params=pltpu.CompilerParams(has_side_effects=EFFECT),
    )(*lands, *sems, after)
    return list(out)


def _blocks_finish(lands, name):
    n = len(lands)

    def body(*refs):
        lnd = refs[n:2 * n]
        send_sems, recv_sems = refs[2 * n:]
        x, y, c, chips = _place()
        sibling = (x, y, 1 - c)

        def copy(k, j, chip_id, pc):
            return pltpu.make_async_remote_copy(
                src_ref=_half_rows(lnd[k], chip_id, pc), dst_ref=_half_rows(lnd[k], chip_id, pc),
                send_sem=send_sems.at[k, j], recv_sem=recv_sems.at[k, j], device_id=sibling, device_id_type=MESH)

        started = []
        for k in range(n):
            for j, (px, py) in enumerate(chips):
                cp = copy(k, j, 2 * px + py, c)
                cp.start()
                started.append(cp)
        for k in range(n):
            for j, (px, py) in enumerate(chips):
                copy(k, j, 2 * px + py, 1 - c).wait_recv()
        for cp in started:
            cp.wait_send()

    out = pl.pallas_call(
        body, name=name,
        out_shape=[jax.ShapeDtypeStruct(l.shape, l.dtype) for l in lands],
        in_specs=[ANY_SPEC] * n, out_specs=[ANY_SPEC] * n,
        input_output_aliases={i: i for i in range(n)},
        scratch_shapes=[pltpu.SemaphoreType.DMA((n, 3)), pltpu.SemaphoreType.DMA((n, 3))],
    )(*lands)
    return list(out)


def _gather_start(shards, name):
    n = len(shards)
    n_sem = 3 * n

    def body(*refs):
        ins, lands = refs[:n], refs[n:2 * n]
        send_sems, recv_sems = refs[2 * n:2 * n + n_sem], refs[2 * n + n_sem:2 * n + 2 * n_sem]
        token = refs[-1]
        x, y, c, chips = _place()
        me_chip = 2 * x + y
        for k in range(n):
            h = shards[k].shape[0] // 2
            rows = pl.ds(pl.multiple_of(c * h, 8), h)
            for j, chip in enumerate(chips):
                pltpu.make_async_remote_copy(
                    src_ref=ins[k].at[rows, :], dst_ref=lands[k].at[me_chip, rows, :],
                    send_sem=send_sems[3 * k + j], recv_sem=recv_sems[3 * k + j],
                    device_id=(*chip, c), device_id_type=MESH).start()
        token[...] = jnp.zeros_like(token)

    lands = [_hbm(lax.empty((N_CHIP,) + s.shape, s.dtype)) for s in shards]
    out = pl.pallas_call(
        body, name=name,
        out_shape=(*[pltpu.SemaphoreType.DMA(())] * (2 * n_sem),
                   *[pltpu.HBM(s.shape, s.dtype) for s in shards],
                   *[pltpu.HBM(l.shape, l.dtype) for l in lands],
                   jax.ShapeDtypeStruct((8, 128), F32)),
        in_specs=[HBM_SPEC] * (2 * n),
        out_specs=(*[SEM_SPEC] * (2 * n_sem), *[HBM_SPEC] * (2 * n), VMEM_SPEC),
        input_output_aliases={i: 2 * n_sem + i for i in range(2 * n)},
        compiler_params=pltpu.CompilerParams(has_side_effects=EFFECT),
    )(*[_hbm(s) for s in shards], *lands)
    sems = list(out[:2 * n_sem])
    return sems, list(out[2 * n_sem:2 * n_sem + n]), list(out[2 * n_sem + n:2 * n_sem + 2 * n]), out[-1]


def _gather_wait(sems, shards, lands, after, name):
    n = len(shards)
    n_sem = 3 * n

    def body(*refs):
        ins, lnd = refs[:n], refs[n:2 * n]
        s_sems, r_sems = refs[2 * n:2 * n + n_sem], refs[2 * n + n_sem:2 * n + 2 * n_sem]
        x, y, c, chips = _place()
        for k in range(n):
            h = shards[k].shape[0] // 2
            rows = pl.ds(pl.multiple_of(c * h, 8), h)
            for j, (px, py) in enumerate(chips):
                cp = pltpu.make_async_remote_copy(
                    src_ref=ins[k].at[rows, :], dst_ref=lnd[k].at[2 * px + py, rows, :],
                    send_sem=s_sems[3 * k + j], recv_sem=r_sems[3 * k + j],
                    device_id=(px, py, c), device_id_type=MESH)
                cp.wait_send()
                cp.wait_recv()

    out = pl.pallas_call(
        body, name=name,
        out_shape=(*[pltpu.HBM(s.shape, s.dtype) for s in shards], *[pltpu.HBM(l.shape, l.dtype) for l in lands]),
        in_specs=[HBM_SPEC] * (2 * n) + [SEM_SPEC] * (2 * n_sem) + [ANY_SPEC],
        out_specs=[HBM_SPEC] * (2 * n),
        input_output_aliases={i: i for i in range(2 * n)},
        compiler_params=pltpu.CompilerParams(has_side_effects=EFFECT),
    )(*shards, *lands, *sems, after)
    return list(out[:n]), list(out[n:])


def _gather_finish(shards, lands, name):
    n = len(shards)

    def body(*refs):
        ins, lnd = refs[:n], refs[2 * n:3 * n]
        send_sems, recv_sems, local_sems = refs[3 * n:]
        x, y, c, chips = _place()
        me_chip = 2 * x + y
        sibling = (x, y, 1 - c)

        def half(k, chip_id, pc):
            h = shards[k].shape[0] // 2
            return lnd[k].at[chip_id, pl.ds(pl.multiple_of(pc * h, 8), h), :]

        def copy(k, j, chip_id, pc):
            return pltpu.make_async_remote_copy(
                src_ref=half(k, chip_id, pc), dst_ref=half(k, chip_id, pc),
                send_sem=send_sems.at[k, j], recv_sem=recv_sems.at[k, j], device_id=sibling, device_id_type=MESH)

        locals_, started = [], []
        for k in range(n):
            cp = pltpu.make_async_copy(ins[k], lnd[k].at[me_chip], local_sems.at[k])
            cp.start()
            locals_.append(cp)
            for j, (px, py) in enumerate(chips):
                cp = copy(k, j, 2 * px + py, c)
                cp.start()
                started.append(cp)
        for k in range(n):
            for j, (px, py) in enumerate(chips):
                copy(k, j, 2 * px + py, 1 - c).wait_recv()
        for cp in started:
            cp.wait_send()
        for cp in locals_:
            cp.wait()

    out = pl.pallas_call(
        body, name=name,
        out_shape=[jax.ShapeDtypeStruct(l.shape, l.dtype) for l in lands],
        in_specs=[ANY_SPEC] * (2 * n), out_specs=[ANY_SPEC] * n,
        input_output_aliases={n + i: i for i in range(n)},
        scratch_shapes=[pltpu.SemaphoreType.DMA((n, 3)), pltpu.SemaphoreType.DMA((n, 3)),
                        pltpu.SemaphoreType.DMA((n,))],
    )(*shards, *lands)
    return list(out)


def _send_other_half(arrs, name):
    n = len(arrs)

    def body(*refs):
        ins, outs = refs[:n], refs[n:2 * n]
        send_sems, recv_sems = refs[2 * n:]
        x, y, c, _ = _place()
        cps = []
        for k in range(n):
            h = arrs[k].shape[1] // 2
            cp = pltpu.make_async_remote_copy(
                src_ref=ins[k].at[:, pl.ds(pl.multiple_of((1 - c) * h, 8), h), :], dst_ref=outs[k],
                send_sem=send_sems.at[k], recv_sem=recv_sems.at[k], device_id=(x, y, 1 - c), device_id_type=MESH)
            cp.start()
            cps.append(cp)
        for cp in cps:
            cp.wait()

    return pl.pallas_call(
        body, name=name,
        out_shape=[jax.ShapeDtypeStruct((a.shape[0], a.shape[1] // 2, a.shape[2]), a.dtype) for a in arrs],
        in_specs=[ANY_SPEC] * n, out_specs=[ANY_SPEC] * n,
        scratch_shapes=[pltpu.SemaphoreType.DMA((n,)), pltpu.SemaphoreType.DMA((n,))],
    )(*arrs)


def _blocks_to_owner(arrs, name):
    n = len(arrs)

    def body(*refs):
        ins, outs = refs[:n], refs[n:2 * n]
        send_sems, recv_sems, local_sems = refs[2 * n:]
        x, y, c, chips = _place()
        me_chip = 2 * x + y
        locals_, started = [], []
        for k in range(n):
            cp = pltpu.make_async_copy(ins[k].at[me_chip], outs[k].at[me_chip], local_sems.at[k])
            cp.start()
            locals_.append(cp)

        def copy(k, j, src_block, dst_slot, to):
            return pltpu.make_async_remote_copy(
                src_ref=ins[k].at[src_block], dst_ref=outs[k].at[dst_slot],
                send_sem=send_sems.at[k, j], recv_sem=recv_sems.at[k, j], device_id=to, device_id_type=MESH)

        for k in range(n):
            for j, (px, py) in enumerate(chips):
                cp = copy(k, j, 2 * px + py, me_chip, (px, py, c))
                cp.start()
                started.append(cp)
        for k in range(n):
            for j, (px, py) in enumerate(chips):
                copy(k, j, me_chip, 2 * px + py, (px, py, c)).wait_recv()
        for cp in started:
            cp.wait_send()
        for cp in locals_:
            cp.wait()

    return pl.pallas_call(
        body, name=name,
        out_shape=[jax.ShapeDtypeStruct(a.shape, a.dtype) for a in arrs],
        in_specs=[ANY_SPEC] * n, out_specs=[ANY_SPEC] * n,
        scratch_shapes=[pltpu.SemaphoreType.DMA((n, 3)), pltpu.SemaphoreType.DMA((n, 3)),
                        pltpu.SemaphoreType.DMA((n,))],
    )(*arrs)


def _scatter_blocks(arrs, name):
    n = len(arrs)

    def body(*refs):
        ins, outs = refs[:n], refs[n:2 * n]
        send_sems, recv_sems = refs[2 * n:]
        x, y, c, chips = _place()
        me_chip = 2 * x + y

        def copy(k, j, src_block, dst_slot, to):
            return pltpu.make_async_remote_copy(
                src_ref=ins[k].at[src_block], dst_ref=outs[k].at[dst_slot],
                send_sem=send_sems.at[k, j], recv_sem=recv_sems.at[k, j], device_id=to, device_id_type=MESH)

        started = []
        for k in range(n):
            for j, (px, py) in enumerate(chips):
                cp = copy(k, j, 2 * px + py, me_chip, (px, py, c))
                cp.start()
                started.append(cp)
        for k in range(n):
            for j, (px, py) in enumerate(chips):
                copy(k, j, me_chip, 2 * px + py, (px, py, c)).wait_recv()
        for cp in started:
            cp.wait_send()

    return pl.pallas_call(
        body, name=name,
        out_shape=[jax.ShapeDtypeStruct(a.shape, a.dtype) for a in arrs],
        in_specs=[ANY_SPEC] * n, out_specs=[ANY_SPEC] * n,
        scratch_shapes=[pltpu.SemaphoreType.DMA((n, 3)), pltpu.SemaphoreType.DMA((n, 3))],
    )(*arrs)


def _scatter_start(arrs, name, after=()):
    n = len(arrs)
    n_sem = 3 * n
    first = 2 * n + len(after)

    def body(*refs):
        ins, lnd = refs[:n], refs[n:2 * n]
        send_sems, recv_sems = refs[first:first + n_sem], refs[first + n_sem:first + 2 * n_sem]
        token = refs[-1]
        x, y, c, chips = _place()
        me_chip = 2 * x + y
        for k in range(n):
            for j, (px, py) in enumerate(chips):
                pltpu.make_async_remote_copy(
                    src_ref=ins[k].at[2 * px + py], dst_ref=lnd[k].at[me_chip],
                    send_sem=send_sems[3 * k + j], recv_sem=recv_sems[3 * k + j],
                    device_id=(px, py, c), device_id_type=MESH).start()
        token[...] = jnp.zeros_like(token)

    lands = [_hbm(lax.empty(a.shape, a.dtype)) for a in arrs]
    out = pl.pallas_call(
        body, name=name,
        out_shape=(*[pltpu.SemaphoreType.DMA(())] * (2 * n_sem),
                   *[pltpu.HBM(a.shape, a.dtype) for a in arrs], *[pltpu.HBM(a.shape, a.dtype) for a in arrs],
                   jax.ShapeDtypeStruct((8, 128), F32)),
        in_specs=[HBM_SPEC] * (2 * n) + [ANY_SPEC] * len(after),
        out_specs=(*[SEM_SPEC] * (2 * n_sem), *[HBM_SPEC] * (2 * n), VMEM_SPEC),
        input_output_aliases={i: 2 * n_sem + i for i in range(2 * n)},
        compiler_params=pltpu.CompilerParams(has_side_effects=EFFECT),
    )(*[_hbm(a) for a in arrs], *lands, *after)
    base = 2 * n_sem
    return list(out[:base]), list(out[base:base + n]), list(out[base + n:base + 2 * n]), out[-1]


def _scatter_wait(sems, arrs, lands, after, name):
    n = len(arrs)
    n_sem = 3 * n

    def body(*refs):
        ins, lnd = refs[:n], refs[n:2 * n]
        s_sems, r_sems = refs[2 * n:2 * n + n_sem], refs[2 * n + n_sem:2 * n + 2 * n_sem]
        x, y, c, chips = _place()
        for k in range(n):
            for j, (px, py) in enumerate(chips):
                cp = pltpu.make_async_remote_copy(
                    src_ref=ins[k].at[2 * px + py], dst_ref=lnd[k].at[2 * px + py],
                    send_sem=s_sems[3 * k + j], recv_sem=r_sems[3 * k + j],
                    device_id=(px, py, c), device_id_type=MESH)
                cp.wait_send()
                cp.wait_recv()

    out = pl.pallas_call(
        body, name=name,
        out_shape=tuple(pltpu.HBM(a.shape, a.dtype) for a in list(arrs) + list(lands)),
        in_specs=[HBM_SPEC] * (2 * n) + [SEM_SPEC] * (2 * n_sem) + [ANY_SPEC] * len(after),
        out_specs=[HBM_SPEC] * (2 * n),
        input_output_aliases={i: i for i in range(2 * n)},
        compiler_params=pltpu.CompilerParams(has_side_effects=EFFECT),
    )(*arrs, *lands, *sems, *after)
    return list(out[:n]), list(out[n:])


def _sum_owner(chip_arr, pairs, got, name):
    nb, h, cols = got.shape
    tr = _row_tile(h, 16, 256)

    def body(chip_ref, own_ref, a_ref, b_ref, c_ref, o_ref):
        o_ref[...] = ((own_ref[0].astype(F32) + a_ref[0].astype(F32)) + b_ref[0].astype(F32)) + c_ref[0].astype(F32)

    def slot(off):
        return pl.BlockSpec((1, tr, cols), lambda i, chip_ref: ((chip_ref[0] + off) % N_CHIP, i, 0))

    return pl.pallas_call(
        body, name=name,
        grid_spec=pltpu.PrefetchScalarGridSpec(
            num_scalar_prefetch=1, grid=(h // tr,),
            in_specs=[slot(0), slot(1), slot(2), slot(3)],
            out_specs=pl.BlockSpec((tr, cols), lambda i, chip_ref: (i, 0))),
        out_shape=jax.ShapeDtypeStruct((h, cols), F32),
        compiler_params=_cparams(dimension_semantics=("parallel",)),
    )(chip_arr, pairs, got, got, got)


def _swap_with_sibling(arrs, name):
    n = len(arrs)

    def body(*refs):
        ins, outs = refs[:n], refs[n:2 * n]
        send_sems, recv_sems = refs[2 * n:]
        x, y, c, _ = _place()
        cps = []
        for k in range(n):
            cp = pltpu.make_async_remote_copy(
                src_ref=ins[k], dst_ref=outs[k], send_sem=send_sems.at[k], recv_sem=recv_sems.at[k],
                device_id=(x, y, 1 - c), device_id_type=MESH)
            cp.start()
            cps.append(cp)
        for cp in cps:
            cp.wait()

    return pl.pallas_call(
        body, name=name,
        out_shape=[jax.ShapeDtypeStruct(a.shape, a.dtype) for a in arrs],
        in_specs=[ANY_SPEC] * n, out_specs=[ANY_SPEC] * n,
        scratch_shapes=[pltpu.SemaphoreType.DMA((n,)), pltpu.SemaphoreType.DMA((n,))],
    )(*arrs)


def _row_tile(h, mult=8, cap=128):
    for t in range(cap - cap % mult, mult - 1, -mult):
        if h % t == 0:
            return t
    raise ValueError(h)


def _cast_bf16(a, name):
    rows, cols = a.shape
    tr = _row_tile(rows, 16, 256)

    def body(a_ref, o_ref):
        o_ref[...] = a_ref[...].astype(BF16)

    return pl.pallas_call(
        body, name=name, grid=(rows // tr,),
        out_shape=jax.ShapeDtypeStruct(a.shape, BF16),
        in_specs=[pl.BlockSpec((tr, cols), lambda i: (i, 0))],
        out_specs=pl.BlockSpec((tr, cols), lambda i: (i, 0)),
        compiler_params=_cparams(dimension_semantics=("parallel",)),
    )(a)


def _pair_sum(c_arr, full, recv, name):
    nb, rows, cols = full.shape
    h = rows // 2
    tr = _row_tile(h, 16, 256)
    steps = h // tr

    def body(c_ref, f_ref, r_ref, o_ref):
        o_ref[...] = (f_ref[...] + r_ref[...]).astype(BF16)

    return pl.pallas_call(
        body, name=name,
        grid_spec=pltpu.PrefetchScalarGridSpec(
            num_scalar_prefetch=1, grid=(nb, steps),
            in_specs=[pl.BlockSpec((1, tr, cols), lambda b, i, c_ref: (b, c_ref[0] * steps + i, 0)),
                      pl.BlockSpec((1, tr, cols), lambda b, i, c_ref: (b, i, 0))],
            out_specs=pl.BlockSpec((1, tr, cols), lambda b, i, c_ref: (b, i, 0))),
        out_shape=jax.ShapeDtypeStruct((nb, h, cols), BF16),
        compiler_params=_cparams(dimension_semantics=("parallel", "parallel")),
    )(c_arr, full, recv)


def _sum_chips(got, name):
    nb, h, cols = got.shape
    tr = _row_tile(h, 16, 256)

    def body(g_ref, o_ref):
        g = g_ref[...].astype(F32)
        o_ref[...] = ((g[0] + g[1]) + g[2]) + g[3]

    return pl.pallas_call(
        body, name=name, grid=(h // tr,),
        out_shape=jax.ShapeDtypeStruct((h, cols), F32),
        in_specs=[pl.BlockSpec((nb, tr, cols), lambda i: (0, i, 0))],
        out_specs=pl.BlockSpec((tr, cols), lambda i: (i, 0)),
        compiler_params=_cparams(dimension_semantics=("parallel",)),
    )(got)


def _adam_math(g, w, m, v):
    m1 = ADAM_B1 * m + (1.0 - ADAM_B1) * g
    v1 = ADAM_B2 * v + (1.0 - ADAM_B2) * (g * g)
    m_hat = m1 / (1.0 - ADAM_B1 ** ADAM_STEP)
    v_hat = v1 / (1.0 - ADAM_B2 ** ADAM_STEP)
    delta = -ADAM_LR * (m_hat / (jnp.sqrt(v_hat) + ADAM_EPS) + ADAM_WD * w)
    return delta, m1, v1


def _adamw_halves(c_arr, own, other, w, m, v, name):
    rows, cols = w.shape
    h = rows // 2
    tr = _row_tile(h)
    steps = h // tr

    def body(c_ref, own_ref, oth_ref, w_ref, m_ref, v_ref, g_out, d_out, m_out, v_out):
        g = jnp.where(pl.program_id(0) == c_ref[0], own_ref[...], oth_ref[...])
        d, m1, v1 = _adam_math(g, w_ref[...], m_ref[...], v_ref[...])
        g_out[...] = g
        d_out[...] = d
        m_out[...] = m1
        v_out[...] = v1

    half_spec = pl.BlockSpec((tr, cols), lambda p, i, c_ref: (i, 0))
    full_spec = pl.BlockSpec((tr, cols), lambda p, i, c_ref: (p * steps + i, 0))
    return pl.pallas_call(
        body, name=name,
        grid_spec=pltpu.PrefetchScalarGridSpec(
            num_scalar_prefetch=1, grid=(2, steps),
            in_specs=[half_spec, half_spec, full_spec, full_spec, full_spec],
            out_specs=[full_spec] * 4),
        out_shape=[jax.ShapeDtypeStruct(w.shape, F32)] * 4,
        compiler_params=_cparams(dimension_semantics=("parallel", "parallel")),
    )(c_arr, own, other, w, m, v)


def _adamw_whole(items, name):
    n = len(items)

    def body(*refs):
        ins, outs = refs[:4 * n], refs[4 * n:]
        for k in range(n):
            g, w, m, v = (r[...] for r in ins[4 * k:4 * k + 4])
            d, m1, v1 = _adam_math(g, w, m, v)
            outs[3 * k][...] = d
            outs[3 * k + 1][...] = m1
            outs[3 * k + 2][...] = v1

    flat = [a for it in items for a in it]
    shapes = [jax.ShapeDtypeStruct(it[1].shape, F32) for it in items for _ in range(3)]
    out = pl.pallas_call(
        body, name=name, out_shape=shapes,
        in_specs=[VMEM_SPEC] * (4 * n), out_specs=[VMEM_SPEC] * (3 * n),
        compiler_params=_cparams(),
    )(*flat)
    return [tuple(out[3 * k:3 * k + 3]) for k in range(n)]


def _adamw_tiled(g, w, m, v, name):
    rows, cols = w.shape
    tr = _row_tile(rows)

    def body(g_ref, w_ref, m_ref, v_ref, d_out, m_out, v_out):
        d, m1, v1 = _adam_math(g_ref[...], w_ref[...], m_ref[...], v_ref[...])
        d_out[...] = d
        m_out[...] = m1
        v_out[...] = v1

    spec = pl.BlockSpec((tr, cols), lambda i: (i, 0))
    return pl.pallas_call(
        body, name=name, grid=(rows // tr,),
        out_shape=[jax.ShapeDtypeStruct(w.shape, F32)] * 3,
        in_specs=[spec] * 4, out_specs=[spec] * 3,
        compiler_params=_cparams(dimension_semantics=("parallel",)),
    )(g, w, m, v)


def _mod_forward(cond, w_mod, b_mod_cols, name):
    def body(c_ref, w_ref, b_ref, o_ref):
        o_ref[...] = _dot(_silu(c_ref[...]), w_ref[...]) + b_ref[...]

    return pl.pallas_call(
        body, name=name, out_shape=jax.ShapeDtypeStruct((cond.shape[0], w_mod.shape[1]), F32),
        in_specs=[VMEM_SPEC] * 3, out_specs=VMEM_SPEC, compiler_params=_cparams(),
    )(cond, w_mod, b_mod_cols)


def _mod_backward(cond, w_mod, dmod_cols, name):
    def body(c_ref, w_ref, d_ref, gw_ref, gc_ref):
        s = _silu(c_ref[...])
        d = d_ref[...]
        gw_ref[...] = _dot_tn(s, d)
        gc_ref[...] = _dot_nt(d[8:16, :], w_ref[...])

    return pl.pallas_call(
        body, name=name,
        out_shape=[jax.ShapeDtypeStruct(w_mod.shape, F32), jax.ShapeDtypeStruct((8, w_mod.shape[0]), F32)],
        in_specs=[VMEM_SPEC] * 3, out_specs=[VMEM_SPEC] * 2, compiler_params=_cparams(),
    )(cond, w_mod, dmod_cols)


def _col_chunks(width, step=512):
    return [(s, min(step, width - s)) for s in range(0, width, step)]


def _in_projection(z, modc, modx, pre1, w_r, n_ctx_tiles, name):
    rows, d = z.shape
    width = w_r.shape[1]

    def body(z_ref, modc_ref, modx_ref, pre_ref, w_ref, h_ref, p_ref):
        is_ctx = pl.program_id(0) < n_ctx_tiles
        n, _ = _rms(z_ref[...])
        shift = jnp.where(is_ctx, modc_ref[0:1, :], modx_ref[0:1, :])
        scale = jnp.where(is_ctx, modc_ref[1:2, :], modx_ref[1:2, :])
        h = (n * pre_ref[...] * (1.0 + scale) + shift).astype(BF16)
        h_ref[...] = h
        for s, w in _col_chunks(width):
            p_ref[:, s:s + w] = jnp.dot(h, w_ref[:, s:s + w], preferred_element_type=F32)

    row = lambda i: (i, 0)
    fixed = lambda i: (0, 0)
    return pl.pallas_call(
        body, name=name, grid=(rows // TM,),
        out_shape=[jax.ShapeDtypeStruct((rows, d), BF16), jax.ShapeDtypeStruct((rows, width), F32)],
        in_specs=[pl.BlockSpec((TM, d), row), pl.BlockSpec((8, d), fixed), pl.BlockSpec((8, d), fixed),
                  pl.BlockSpec((1, d), fixed), VMEM_SPEC],
        out_specs=[pl.BlockSpec((TM, d), row), pl.BlockSpec((TM, width), row)],
        compiler_params=_cparams(dimension_semantics=("parallel",)),
    )(z, modc, modx, pre1, w_r)


C_HQ, C_HI, C_HF_FW, C_HF_BW, C_HGATE, C_GQ, C_GK, C_GV, C_GGATE = range(9)
OFF_GATE_HG = 9 * HW
OFF_LR = 13 * HW
P_WIDTH = OFF_LR + 128


def _head_norm_fwd(o, w):
    outs, ns, rs = [], [], []
    for h in range(NH):
        n, r = _rms(o[:, h * HD:(h + 1) * HD])
        ns.append(n)
        rs.append(r)
        outs.append(n * w)
    return jnp.concatenate(outs, axis=1), ns, rs


def _mixer_tail(z, o_hg, o_gla, p_hgate, p_ggate, p_gate_hg, p_gate_gla, hg_on, gla_on, wbh, wbg, wout):
    on_hg, n_hg, r_hg = _head_norm_fwd(o_hg, hg_on)
    on_gla, n_gla, r_gla = _head_norm_fwd(o_gla, gla_on)
    og_hg = (on_hg * _silu(p_hgate)).astype(BF16)
    og_gla = (on_gla * _silu(p_ggate)).astype(BF16)
    b_hg = jnp.dot(og_hg, wbh, preferred_element_type=F32)
    b_gla = jnp.dot(og_gla, wbg, preferred_element_type=F32)
    s_hg = _sigmoid(p_gate_hg)
    s_gla = _sigmoid(p_gate_gla)
    merged = (s_hg * b_hg + s_gla * b_gla).astype(BF16)
    y1 = jnp.dot(merged, wout, preferred_element_type=F32)
    return dict(on_hg=on_hg, n_hg=n_hg, r_hg=r_hg, on_gla=on_gla, n_gla=n_gla, r_gla=r_gla, og_hg=og_hg,
                og_gla=og_gla, b_hg=b_hg, b_gla=b_gla, s_hg=s_hg, s_gla=s_gla, merged=merged, y1=y1)


def _mixer_ffn(x_lat, p, o_list, modx, norms, onorms, w_br_hg, w_br_gla, w_out, w_gate, w_up, w_down, target,
               n_ctx_tiles, name):
    rows, d = x_lat.shape
    dff = w_gate.shape[1]
    inv_d = 1.0 / d

    def body(x_ref, ofw_hg, obw_hg, ofw_gla, obw_gla, p_hgate, p_ggate, p_ghg_a, p_ghg_b, p_ggla_a, p_ggla_b,
             modx_ref, norm_ref, on_ref, wbh_ref, wbg_ref, wout_ref, wg_ref, wu_ref, wd_ref, t_ref,
             loss_ref, dz2_ref, y1_ref, mrg_ref, oghg_ref, oggla_ref, h2_ref, a_ref, du_ref, dv_ref, dy2_ref,
             stat_ref):
        i = pl.program_id(0)
        post1, pre2, post2 = norm_ref[1:2, :], norm_ref[2:3, :], norm_ref[3:4, :]
        gate1, shift2, scale2, gate2 = modx_ref[2:3, :], modx_ref[3:4, :], modx_ref[4:5, :], modx_ref[5:6, :]
        p_gate_hg = jnp.concatenate([p_ghg_a[...], p_ghg_b[...]], axis=1)
        p_gate_gla = jnp.concatenate([p_ggla_a[...], p_ggla_b[...]], axis=1)
        t = _mixer_tail(x_ref[...], ofw_hg[...] + obw_hg[...], ofw_gla[...] + obw_gla[...], p_hgate[...],
                        p_ggate[...], p_gate_hg, p_gate_gla, on_ref[0:1, 0:HD], on_ref[1:2, 0:HD],
                        wbh_ref[...], wbg_ref[...], wout_ref[...])
        y1_ref[...] = t["y1"]
        mrg_ref[...] = t["merged"]
        oghg_ref[...] = t["og_hg"]
        oggla_ref[...] = t["og_gla"]
        n1, _ = _rms(t["y1"])
        z2 = x_ref[...] + n1 * post1 * gate1
        n2, r2 = _rms(z2)
        nw2 = n2 * pre2
        h2 = (nw2 * (1.0 + scale2) + shift2).astype(BF16)
        h2_ref[...] = h2
        u = jnp.dot(h2, wg_ref[...], preferred_element_type=F32)
        v = jnp.dot(h2, wu_ref[...], preferred_element_type=F32)
        su = _silu(u)
        a = (su * v).astype(BF16)
        a_ref[...] = a
        y2 = jnp.dot(a, wd_ref[...], preferred_element_type=F32)
        n3, r3 = _rms(y2)
        z3 = z2 + n3 * post2 * gate2
        err = z3 - t_ref[...]
        part = 0.5 * inv_d * jnp.sum(err * err)
        dz3 = err * inv_d
        dgate2 = _colsum(dz3 * n3 * post2)
        tt = dz3 * gate2
        dpost2 = _colsum(tt * n3)
        dy2 = _rms_bwd(tt * post2, n3, r3).astype(BF16)
        dy2_ref[...] = dy2
        da = _dot_nt(dy2, wd_ref[...])
        du = (da * v * _dsilu(u)).astype(BF16)
        dv = (da * su).astype(BF16)
        du_ref[...] = du
        dv_ref[...] = dv
        dh2 = _dot_nt(du, wg_ref[...]) + _dot_nt(dv, wu_ref[...])
        dshift2 = _colsum(dh2)
        dscale2 = _colsum(dh2 * nw2)
        dnw2 = dh2 * (1.0 + scale2)
        dpre2 = _colsum(dnw2 * n2)
        dz2_ref[...] = dz3 + _rms_bwd(dnw2 * pre2, n2, r2)

        @pl.when(i == 0)
        def _():
            stat_ref[...] = jnp.zeros_like(stat_ref)
            loss_ref[...] = jnp.zeros_like(loss_ref)

        for r, val in enumerate((dshift2, dscale2, dgate2, dpre2, dpost2)):
            stat_ref[r:r + 1, :] += val
        loss_ref[...] += part

    tm = TM_FFN
    ctx_tiles = n_ctx_tiles * (TM // tm)
    lat = lambda i: (i, 0)
    full = lambda i: (i + ctx_tiles, 0)
    fixed = lambda i: (0, 0)

    def pcol(blk):
        return pl.BlockSpec((tm, HW), lambda i: (i + ctx_tiles, blk))

    in_specs = ([pl.BlockSpec((tm, d), lat)] + [pl.BlockSpec((tm, HW), full)] * 4
                + [pcol(C_HGATE), pcol(C_GGATE), pcol(9), pcol(10), pcol(11), pcol(12)]
                + [pl.BlockSpec((8, d), fixed), pl.BlockSpec((8, d), fixed), pl.BlockSpec((8, d), fixed)]
                + [VMEM_SPEC] * 6 + [pl.BlockSpec((tm, d), lat)])
    bf = lambda w: jax.ShapeDtypeStruct((rows, w), BF16)
    out_shape = [jax.ShapeDtypeStruct((8, 128), F32), jax.ShapeDtypeStruct((rows, d), F32),
                 jax.ShapeDtypeStruct((rows, d), F32), bf(d), bf(HW), bf(HW), bf(d), bf(dff), bf(dff), bf(dff), bf(d),
                 jax.ShapeDtypeStruct((8, d), F32)]
    out_specs = [pl.BlockSpec((8, 128), fixed), pl.BlockSpec((tm, d), lat), pl.BlockSpec((tm, d), lat),
                 pl.BlockSpec((tm, d), lat), pl.BlockSpec((tm, HW), lat), pl.BlockSpec((tm, HW), lat),
                 pl.BlockSpec((tm, d), lat), pl.BlockSpec((tm, dff), lat), pl.BlockSpec((tm, dff), lat),
                 pl.BlockSpec((tm, dff), lat), pl.BlockSpec((tm, d), lat), pl.BlockSpec((8, d), fixed)]
    return pl.pallas_call(
        body, name=name, grid=(rows // tm,), out_shape=out_shape, in_specs=in_specs, out_specs=out_specs,
        compiler_params=_cparams(dimension_semantics=("arbitrary",)),
    )(x_lat, *o_list, p, p, p, p, p, p, modx, norms, onorms, w_br_hg, w_br_gla, w_out, w_gate, w_up, w_down, target)


def _mixer_tail_fwd(x_lat, p, o_list, modx, norms, onorms, w_br_hg, w_br_gla, w_out, n_ctx_tiles, name):
    rows, d = x_lat.shape

    def body(x_ref, ofw_hg, obw_hg, ofw_gla, obw_gla, p_hgate, p_ggate, p_ghg_a, p_ghg_b, p_ggla_a, p_ggla_b,
             modx_ref, norm_ref, on_ref, wbh_ref, wbg_ref, wout_ref, z2_ref, y1_ref, mrg_ref, oghg_ref, oggla_ref):
        p_gate_hg = jnp.concatenate([p_ghg_a[...], p_ghg_b[...]], axis=1)
        p_gate_gla = jnp.concatenate([p_ggla_a[...], p_ggla_b[...]], axis=1)
        t = _mixer_tail(x_ref[...], ofw_hg[...] + obw_hg[...], ofw_gla[...] + obw_gla[...], p_hgate[...],
                        p_ggate[...], p_gate_hg, p_gate_gla, on_ref[0:1, 0:HD], on_ref[1:2, 0:HD],
                        wbh_ref[...], wbg_ref[...], wout_ref[...])
        y1_ref[...] = t["y1"]
        mrg_ref[...] = t["merged"]
        oghg_ref[...] = t["og_hg"]
        oggla_ref[...] = t["og_gla"]
        n1, _ = _rms(t["y1"])
        z2_ref[...] = x_ref[...] + n1 * norm_ref[1:2, :] * modx_ref[2:3, :]

    lat = lambda i: (i, 0)
    full = lambda i: (i + n_ctx_tiles, 0)
    fixed = lambda i: (0, 0)

    def pcol(blk):
        return pl.BlockSpec((TM, HW), lambda i: (i + n_ctx_tiles, blk))

    in_specs = ([pl.BlockSpec((TM, d), lat)] + [pl.BlockSpec((TM, HW), full)] * 4
                + [pcol(C_HGATE), pcol(C_GGATE), pcol(9), pcol(10), pcol(11), pcol(12)]
                + [pl.BlockSpec((8, d), fixed)] * 3 + [VMEM_SPEC] * 3)
    bf = lambda w: jax.ShapeDtypeStruct((rows, w), BF16)
    f32 = jax.ShapeDtypeStruct((rows, d), F32)
    return pl.pallas_call(
        body, name=name, grid=(rows // TM,), out_shape=[f32, f32, bf(d), bf(HW), bf(HW)], in_specs=in_specs,
        out_specs=[pl.BlockSpec((TM, d), lat)] * 3 + [pl.BlockSpec((TM, HW), lat)] * 2,
        compiler_params=_cparams(dimension_semantics=("parallel",)),
    )(x_lat, *o_list, p, p, p, p, p, p, modx, norms, onorms, w_br_hg, w_br_gla, w_out)


def _ffn_fwd_bwd(z2, modx, norms, w_gate, w_up, w_down, target, name):
    rows, d = z2.shape
    dff = w_gate.shape[1]
    inv_d = 1.0 / d

    def body(z2_ref, modx_ref, norm_ref, wg_ref, wu_ref, wd_ref, t_ref,
             loss_ref, dz2_ref, h2_ref, a_ref, du_ref, dv_ref, dy2_ref, stat_ref):
        i = pl.program_id(0)
        pre2, post2 = norm_ref[2:3, :], norm_ref[3:4, :]
        shift2, scale2, gate2 = modx_ref[3:4, :], modx_ref[4:5, :], modx_ref[5:6, :]
        z2 = z2_ref[...]
        n2, r2 = _rms(z2)
        nw2 = n2 * pre2
        h2 = (nw2 * (1.0 + scale2) + shift2).astype(BF16)
        h2_ref[...] = h2
        u = jnp.dot(h2, wg_ref[...], preferred_element_type=F32)
        v = jnp.dot(h2, wu_ref[...], preferred_element_type=F32)
        su = _silu(u)
        a = (su * v).astype(BF16)
        a_ref[...] = a
        y2 = jnp.dot(a, wd_ref[...], preferred_element_type=F32)
        n3, r3 = _rms(y2)
        err = z2 + n3 * post2 * gate2 - t_ref[...]
        part = 0.5 * inv_d * jnp.sum(err * err)
        dz3 = err * inv_d
        dgate2 = _colsum(dz3 * n3 * post2)
        tt = dz3 * gate2
        dpost2 = _colsum(tt * n3)
        dy2 = _rms_bwd(tt * post2, n3, r3).astype(BF16)
        dy2_ref[...] = dy2
        da = _dot_nt(dy2, wd_ref[...])
        du = (da * v * _dsilu(u)).astype(BF16)
        dv = (da * su).astype(BF16)
        du_ref[...] = du
        dv_ref[...] = dv
        dh2 = _dot_nt(du, wg_ref[...]) + _dot_nt(dv, wu_ref[...])
        dshift2 = _colsum(dh2)
        dscale2 = _colsum(dh2 * nw2)
        dnw2 = dh2 * (1.0 + scale2)
        dpre2 = _colsum(dnw2 * n2)
        dz2_ref[...] = dz3 + _rms_bwd(dnw2 * pre2, n2, r2)

        @pl.when(i == 0)
        def _():
            stat_ref[...] = jnp.zeros_like(stat_ref)
            loss_ref[...] = jnp.zeros_like(loss_ref)

        for r, val in enumerate((dshift2, dscale2, dgate2, dpre2, dpost2)):
            stat_ref[r:r + 1, :] += val
        loss_ref[...] += part

    lat = lambda i: (i, 0)
    fixed = lambda i: (0, 0)
    bf = lambda w: jax.ShapeDtypeStruct((rows, w), BF16)
    return pl.pallas_call(
        body, name=name, grid=(rows // TM,),
        out_shape=[jax.ShapeDtypeStruct((8, 128), F32), jax.ShapeDtypeStruct((rows, d), F32), bf(d), bf(dff), bf(dff),
                   bf(dff), bf(d), jax.ShapeDtypeStruct((8, d), F32)],
        in_specs=[pl.BlockSpec((TM, d), lat), pl.BlockSpec((8, d), fixed), pl.BlockSpec((8, d), fixed)]
        + [VMEM_SPEC] * 3 + [pl.BlockSpec((TM, d), lat)],
        out_specs=[pl.BlockSpec((8, 128), fixed), pl.BlockSpec((TM, d), lat), pl.BlockSpec((TM, d), lat),
                   pl.BlockSpec((TM, dff), lat), pl.BlockSpec((TM, dff), lat), pl.BlockSpec((TM, dff), lat),
                   pl.BlockSpec((TM, d), lat), pl.BlockSpec((8, d), fixed)],
        compiler_params=_cparams(dimension_semantics=("arbitrary",)),
    )(z2, modx, norms, w_gate, w_up, w_down, target)


def _mixer_tail_bwd(x_lat, p, o_list, dz2, y1, modx, norms, onorms, w_br_hg, w_br_gla, w_out, n_ctx_tiles, n_tiles,
                    name):
    rows, d = x_lat.shape
    total = n_tiles * TM

    def body(x_ref, ofw_hg, obw_hg, ofw_gla, obw_gla, p_hgate, p_ggate, p_ghg_a, p_ghg_b, p_ggla_a, p_ggla_b,
             dz2_ref, y1_ref, modx_ref, norm_ref, on_ref, wbh_ref, wbg_ref, wout_ref,
             dohg_ref, dogla_ref, dhgate_ref, dggate_ref, dghg_ref, dggla_ref, dy1_ref, dbhg_ref, dbgla_ref,
             stat_ref):
        i = pl.program_id(0)

        @pl.when(i == 0)
        def _():
            stat_ref[...] = jnp.zeros_like(stat_ref)

        @pl.when(i < n_ctx_tiles)
        def _():
            for ref in (dohg_ref, dogla_ref, dhgate_ref, dggate_ref, dghg_ref, dggla_ref):
                ref[...] = jnp.zeros_like(ref)

        @pl.when(i >= n_ctx_tiles)
        def _():
            post1, gate1 = norm_ref[1:2, :], modx_ref[2:3, :]
            hg_on, gla_on = on_ref[0:1, 0:HD], on_ref[1:2, 0:HD]
            p_gate_hg = jnp.concatenate([p_ghg_a[...], p_ghg_b[...]], axis=1)
            p_gate_gla = jnp.concatenate([p_ggla_a[...], p_ggla_b[...]], axis=1)
            ph, pg = p_hgate[...], p_ggate[...]
            t = _mixer_tail(x_ref[...], ofw_hg[...] + obw_hg[...], ofw_gla[...] + obw_gla[...], ph, pg,
                            p_gate_hg, p_gate_gla, hg_on, gla_on, wbh_ref[...], wbg_ref[...], wout_ref[...])
            dz2 = dz2_ref[...]
            n1, r1 = _rms(y1_ref[...])
            dgate1 = _colsum(dz2 * n1 * post1)
            tt = dz2 * gate1
            dpost1 = _colsum(tt * n1)
            dy1 = _rms_bwd(tt * post1, n1, r1).astype(BF16)
            dy1_ref[...] = dy1
            dmerged = _dot_nt(dy1, wout_ref[...])
            dghg_ref[...] = dmerged * t["b_hg"] * t["s_hg"] * (1.0 - t["s_hg"])
            dggla_ref[...] = dmerged * t["b_gla"] * t["s_gla"] * (1.0 - t["s_gla"])
            db_hg = (dmerged * t["s_hg"]).astype(BF16)
            db_gla = (dmerged * t["s_gla"]).astype(BF16)
            dbhg_ref[...] = db_hg
            dbgla_ref[...] = db_gla
            don_acc = []
            for (db, wb, pgate, on, ns, rs, gain, gate_ref, do_ref) in (
                    (db_hg, wbh_ref, ph, t["on_hg"], t["n_hg"], t["r_hg"], hg_on, dhgate_ref, dohg_ref),
                    (db_gla, wbg_ref, pg, t["on_gla"], t["n_gla"], t["r_gla"], gla_on, dggate_ref, dogla_ref)):
                dog = _dot_nt(db, wb[...])
                gate_ref[...] = dog * on * _dsilu(pgate)
                don = dog * _silu(pgate)
                acc = jnp.zeros((1, HD), F32)
                for h in range(NH):
                    sl = slice(h * HD, (h + 1) * HD)
                    acc = acc + _colsum(don[:, sl] * ns[h])
                    do_ref[:, sl] = _rms_bwd(don[:, sl] * gain, ns[h], rs[h])
                don_acc.append(acc)
            stat_ref[0:1, :] += dgate1
            stat_ref[1:2, :] += dpost1
            stat_ref[2:3, 0:HD] += don_acc[0]
            stat_ref[2:3, HD:2 * HD] += don_acc[1]

    lat = lambda i: (jnp.maximum(i - n_ctx_tiles, 0), 0)
    full = lambda i: (i, 0)
    fixed = lambda i: (0, 0)

    def pcol(blk):
        return pl.BlockSpec((TM, HW), lambda i: (i, blk))

    in_specs = ([pl.BlockSpec((TM, d), lat)] + [pl.BlockSpec((TM, HW), full)] * 4
                + [pcol(C_HGATE), pcol(C_GGATE), pcol(9), pcol(10), pcol(11), pcol(12)]
                + [pl.BlockSpec((TM, d), lat), pl.BlockSpec((TM, d), lat)]
                + [pl.BlockSpec((8, d), fixed)] * 3 + [VMEM_SPEC] * 3)
    f = lambda w: jax.ShapeDtypeStruct((total, w), F32)
    out_shape = [f(HW), f(HW), f(HW), f(HW), f(d), f(d), jax.ShapeDtypeStruct((rows, d), BF16),
                 jax.ShapeDtypeStruct((rows, d), BF16), jax.ShapeDtypeStruct((rows, d), BF16),
                 jax.ShapeDtypeStruct((8, d), F32)]
    out_specs = ([pl.BlockSpec((TM, HW), full)] * 4 + [pl.BlockSpec((TM, d), full)] * 2
                 + [pl.BlockSpec((TM, d), lat)] * 3 + [pl.BlockSpec((8, d), fixed)])
    return pl.pallas_call(
        body, name=name, grid=(n_tiles,), out_shape=out_shape, in_specs=in_specs, out_specs=out_specs,
        compiler_params=_cparams(dimension_semantics=("arbitrary",)),
    )(x_lat, *o_list, p, p, p, p, p, p, dz2, y1, modx, norms, onorms, w_br_hg, w_br_gla, w_out)


def _in_projection_bwd(z, dz2, modc, modx, pre1, w_r, pieces, n_ctx_tiles, name):
    rows, d = z.shape
    lat_rows = dz2.shape[0]
    width = w_r.shape[1]
    n_pieces = len(pieces)

    def body(*refs):
        z_ref, dz2_ref, modc_ref, modx_ref, pre_ref, w_ref = refs[:6]
        (dhq_f, dhq_b, dhi_f, dhi_b, dhf_f, dhf_b, dhgate, dgq_f, dgq_b, dgk_f, dgk_b, dgv_f, dgv_b, dggate,
         dghg, dggla, dlr_f, dlr_b) = refs[6:6 + n_pieces]
        dp_ref, gx_ref, stat_ref = refs[6 + n_pieces:]
        i = pl.program_id(0)
        is_ctx = i < n_ctx_tiles
        sections = [
            (0, dhq_f[...] + dhq_b[...]), (HW, dhi_f[...] + dhi_b[...]), (2 * HW, dhf_f[...]), (3 * HW, dhf_b[...]),
            (4 * HW, dhgate[...]), (5 * HW, dgq_f[...] + dgq_b[...]), (6 * HW, dgk_f[...] + dgk_b[...]),
            (7 * HW, dgv_f[...] + dgv_b[...]), (8 * HW, dggate[...]),
            (9 * HW, dghg[:, 0:HW]), (10 * HW, dghg[:, HW:2 * HW]),
            (11 * HW, dggla[:, 0:HW]), (12 * HW, dggla[:, HW:2 * HW]), (OFF_LR, dlr_f[...] + dlr_b[...])]
        dh = jnp.zeros((TM, d), F32)
        for off, val in sections:
            w = val.shape[1]
            vb = val.astype(BF16)
            dp_ref[:, off:off + w] = vb
            dh = dh + _dot_nt(vb, w_ref[:, off:off + w])
        n, r = _rms(z_ref[...])
        pre = pre_ref[...]
        scale = jnp.where(is_ctx, modc_ref[1:2, :], modx_ref[1:2, :])
        nw = n * pre
        dshift = _colsum(dh)
        dscale = _colsum(dh * nw)
        dnw = dh * (1.0 + scale)
        dpre = _colsum(dnw * n)
        gx_ref[...] = dz2_ref[...] + _rms_bwd(dnw * pre, n, r)
        zero = jnp.zeros((1, d), F32)

        @pl.when(i == 0)
        def _():
            stat_ref[...] = jnp.zeros_like(stat_ref)

        stat_ref[0:1, :] += jnp.where(is_ctx, zero, dshift)
        stat_ref[1:2, :] += jnp.where(is_ctx, zero, dscale)
        stat_ref[2:3, :] += jnp.where(is_ctx, dshift, zero)
        stat_ref[3:4, :] += jnp.where(is_ctx, dscale, zero)
        stat_ref[4:5, :] += dpre

    full = lambda i: (i, 0)
    lat = lambda i: (jnp.maximum(i - n_ctx_tiles, 0), 0)
    fixed = lambda i: (0, 0)
    piece_specs = [pl.BlockSpec((TM, a.shape[1]), full) for a in pieces]
    in_specs = [pl.BlockSpec((TM, d), full), pl.BlockSpec((TM, d), lat), pl.BlockSpec((8, d), fixed),
                pl.BlockSpec((8, d), fixed), pl.BlockSpec((1, d), fixed), VMEM_SPEC] + piece_specs
    return pl.pallas_call(
        body, name=name, grid=(rows // TM,),
        out_shape=[jax.ShapeDtypeStruct((rows, width), BF16), jax.ShapeDtypeStruct((lat_rows, d), F32),
                   jax.ShapeDtypeStruct((8, d), F32)],
        in_specs=in_specs,
        out_specs=[pl.BlockSpec((TM, width), full), pl.BlockSpec((TM, d), lat), pl.BlockSpec((8, d), fixed)],
        compiler_params=_cparams(dimension_semantics=("arbitrary",)),
    )(z, dz2, modc, modx, pre1, w_r, *pieces)


def _transposed_lhs_matmul(x_ref, dy_ref, o_ref, xt_ref):
    @pl.when(pl.program_id(1) == 0)
    def _():
        xt_ref[...] = x_ref[...].T

    o_ref[...] = jnp.dot(xt_ref[...], dy_ref[...], preferred_element_type=F32)


def _weight_grad(xs, dy, name, col_block=None, tk=None, tn=512):
    rows = dy.shape[0]
    k = xs.shape[1]
    if col_block is None:
        n, tn_, cb = dy.shape[1], tn, 0
    else:
        n, tn_, cb = col_block[0], col_block[0], col_block[1]
    tn_ = min(tn_, n)
    tk_ = k if tk is None else tk

    return pl.pallas_call(
        functools.partial(_transposed_lhs_matmul), name=name, grid=(k // tk_, n // tn_),
        out_shape=jax.ShapeDtypeStruct((k, n), F32),
        in_specs=[pl.BlockSpec((rows, tk_), lambda i, j: (0, i)),
                  pl.BlockSpec((rows, tn_), lambda i, j: (0, j + cb))],
        out_specs=pl.BlockSpec((tk_, tn_), lambda i, j: (i, j)),
        scratch_shapes=[pltpu.VMEM((tk_, rows), BF16)],
        compiler_params=_cparams(dimension_semantics=("parallel", "arbitrary")),
    )(xs, dy)


def _running_sum(x, fw):
    c = x.shape[0]
    row = lax.broadcasted_iota(jnp.int32, (c, 1), 0)
    s = 1
    while s < c:
        if fw:
            x = x + jnp.where(row >= s, pltpu.roll(x, s, axis=0), 0.0)
        else:
            x = x + jnp.where(row < c - s, pltpu.roll(x, c - s, axis=0), 0.0)
        s *= 2
    return x


def _chunk_terms(q, k, g, fw):
    c = CHUNK
    r = lax.broadcasted_iota(jnp.int32, (c, c), 0)
    s = lax.broadcasted_iota(jnp.int32, (c, c), 1)
    causal = (s <= r) if fw else (s >= r)
    causal_t = (s >= r) if fw else (s <= r)
    cum = _running_sum(g, fw)
    row = lax.broadcasted_iota(jnp.int32, (c, 1), 0)
    pos = row if fw else (c - 1 - row)
    starts = [None]
    for j in range(1, NSUB):
        rj = SUB * j - 1 if fw else c - SUB * j
        starts.append(cum[rj:rj + 1, :])
    in_blk = [(pos >= SUB * j) & (pos < SUB * (j + 1)) for j in range(NSUB)]
    e = [jnp.exp(cum)]
    for j in range(1, NSUB):
        e.append(jnp.exp(jnp.where(pos >= SUB * j, cum - starts[j], -1e30)))
    own = jnp.zeros_like(cum)
    for j in range(1, NSUB):
        own = own + jnp.where(in_blk[j], starts[j], 0.0)
    kscale = jnp.exp(own - cum)
    rend = c - 1 if fw else 0
    cend = cum[rend:rend + 1, :]
    tail = jnp.exp(cend - cum)
    qcat = jnp.concatenate([q * e[j] for j in range(NSUB)], axis=1).astype(BF16)
    kt = k * kscale
    km = jnp.concatenate([jnp.where(in_blk[j], kt, 0.0) for j in range(NSUB)], axis=1).astype(BF16)
    return dict(causal=causal, causal_t=causal_t, e=e, in_blk=in_blk, kscale=kscale, cend=cend, tail=tail,
                qcat=qcat, km=km)


def _chunk_fwd(q, k, v, g, st0, fw):
    t = _chunk_terms(q, k, g, fw)
    a = jnp.where(t["causal"], _dot_nt(t["qcat"], t["km"]), 0.0)
    o = _dot(a, v) + _dot_nt(t["qcat"][:, 0:HD], st0)
    st1 = st0 * jnp.exp(t["cend"]) + _dot_tn(v, k * t["tail"])
    return o, st1


def _chunk_bwd(q, k, v, g, st0, do, dst1, fw):
    t = _chunk_terms(q, k, g, fw)
    qcat, km, e = t["qcat"], t["km"], t["e"]
    a_t = jnp.where(t["causal_t"], _dot_nt(km, qcat), 0.0)
    ktail = k * t["tail"]
    dv = _dot(a_t, do) + _dot_nt(ktail, dst1)
    da = jnp.where(t["causal"], _dot_nt(do, v), 0.0)
    da_t = jnp.where(t["causal_t"], _dot_nt(v, do), 0.0)
    dqcat = _dot(da, km)
    dq_inter = e[0] * _dot(do, st0)
    dq = dq_inter
    for j in range(NSUB):
        dq = dq + e[j] * dqcat[:, j * HD:(j + 1) * HD]
    dkm = _dot(da_t, qcat)
    dkt = jnp.zeros_like(k)
    for j in range(NSUB):
        dkt = dkt + jnp.where(t["in_blk"][j], dkm[:, j * HD:(j + 1) * HD], 0.0)
    dk_inter = _dot(v, dst1) * t["tail"]
    dk = dkt * t["kscale"] + dk_inter
    dcum = q * dq_inter - k * dk_inter
    for j in range(NSUB):
        sl = slice(j * HD, (j + 1) * HD)
        dcum = dcum + qcat[:, sl].astype(F32) * dqcat[:, sl] - km[:, sl].astype(F32) * dkm[:, sl]
    ecend = jnp.exp(t["cend"])
    end = ecend * _colsum(st0 * dst1) + _colsum(k * dk_inter)
    dg = _running_sum(dcum, not fw) + end
    dst0 = dst1 * ecend + _dot_tn(do, q * e[0])
    return dq, dk, dv, dg, dst0


def _chunk_index(step, n_ctx_chunks, n_chunks, fw):
    if fw:
        return step
    return jnp.where(step < n_ctx_chunks, n_ctx_chunks - 1 - step, n_chunks - 1 + n_ctx_chunks - step)


def _hg_inputs(hq, hf, lbv, d_idx, sl):
    lb = _sigmoid(lbv[d_idx:d_idx + 1, sl] - lbv[2 + d_idx:3 + d_idx, sl])
    sg = _sigmoid(hf)
    f = lb + (1.0 - lb) * sg
    return _silu(hq), 1.0 - f, jnp.log(f), f, sg, lb


def _scan_fwd(p, side, n_ctx_chunks, fw, branch, name):
    rows = p.shape[0]
    n_chunks = rows // CHUNK
    d_idx = 0 if fw else 1
    hg = branch == "hg"
    cols = (C_HQ, C_HI, C_HF_FW + d_idx) if hg else (C_GQ, C_GK, C_GV)

    def body(*refs):
        if hg:
            a_ref, b_ref, c_ref, lb_ref, o_ref, st_ref, state = refs
        else:
            a_ref, b_ref, c_ref, lr_ref, wgk_ref, bgk_ref, o_ref, st_ref, state = refs
            logits = _dot(lr_ref[...], wgk_ref[...]) + bgk_ref[...]
            g_all = _log_sigmoid(logits) * (1.0 / GATE_NORM)

        @pl.when(pl.program_id(0) == 0)
        def _():
            state[...] = jnp.zeros_like(state)

        for h in range(NH):
            sl = slice(h * HD, (h + 1) * HD)
            if hg:
                q, k, g, _, _, _ = _hg_inputs(a_ref[:, sl], c_ref[:, sl], lb_ref[...], d_idx, sl)
                v = b_ref[:, sl]
            else:
                q, k, v, g = a_ref[:, sl] * (HD ** -0.5), b_ref[:, sl], c_ref[:, sl], g_all[:, sl]
            st0 = state[h]
            st_ref[0, h] = st0
            o, st1 = _chunk_fwd(q, k, v, g, st0, fw)
            o_ref[:, sl] = o
            state[h] = st1

    def cmap(blk):
        return pl.BlockSpec((CHUNK, HW), lambda j: (_chunk_index(j, n_ctx_chunks, n_chunks, fw), blk))

    fixed = lambda j: (0, 0)
    in_specs = [cmap(cols[0]), cmap(cols[1]), cmap(cols[2])]
    if hg:
        in_specs += [pl.BlockSpec((4, HW), fixed)]
        args = (p, p, p, side)
    else:
        in_specs += [pl.BlockSpec((CHUNK, 128), lambda j: (_chunk_index(j, n_ctx_chunks, n_chunks, fw), OFF_LR // 128)),
                     pl.BlockSpec((128, HW), fixed), pl.BlockSpec((1, HW), fixed)]
        args = (p, p, p, p, side[0], side[1])
    return pl.pallas_call(
        body, name=name, grid=(n_chunks,),
        out_shape=[jax.ShapeDtypeStruct((rows, HW), F32), jax.ShapeDtypeStruct((n_chunks, NH, HD, HD), F32)],
        in_specs=in_specs,
        out_specs=[pl.BlockSpec((CHUNK, HW), lambda j: (_chunk_index(j, n_ctx_chunks, n_chunks, fw), 0)),
                   pl.BlockSpec((1, NH, HD, HD), lambda j: (_chunk_index(j, n_ctx_chunks, n_chunks, fw), 0, 0, 0))],
        scratch_shapes=[pltpu.VMEM((NH, HD, HD), F32)],
        compiler_params=_cparams(dimension_semantics=("arbitrary",)),
    )(*args)


def _scan_bwd(p, side, states, d_o, n_ctx_chunks, fw, branch, name):
    rows = p.shape[0]
    n_chunks = rows // CHUNK
    d_idx = 0 if fw else 1
    hg = branch == "hg"
    cols = (C_HQ, C_HI, C_HF_FW + d_idx) if hg else (C_GQ, C_GK, C_GV)

    def body(*refs):
        if hg:
            a_ref, b_ref, c_ref, lb_ref, st_ref, do_ref, da_ref, db_ref, dc_ref, dlb_ref, dstate = refs
        else:
            (a_ref, b_ref, c_ref, lr_ref, wgk_ref, bgk_ref, st_ref, do_ref, da_ref, db_ref, dc_ref, dlr_ref,
             dwgk_ref, dbias_ref, dstate) = refs
            lr = lr_ref[...]
            logits = _dot(lr, wgk_ref[...]) + bgk_ref[...]
            g_all = _log_sigmoid(logits) * (1.0 / GATE_NORM)

        @pl.when(pl.program_id(0) == 0)
        def _():
            dstate[...] = jnp.zeros_like(dstate)
            if hg:
                dlb_ref[...] = jnp.zeros_like(dlb_ref)
            else:
                dwgk_ref[...] = jnp.zeros_like(dwgk_ref)
                dbias_ref[...] = jnp.zeros_like(dbias_ref)

        dg_parts = []
        for h in range(NH):
            sl = slice(h * HD, (h + 1) * HD)
            if hg:
                hq, hf = a_ref[:, sl], c_ref[:, sl]
                q, k, g, f, sg, lb = _hg_inputs(hq, hf, lb_ref[...], d_idx, sl)
                v = b_ref[:, sl]
            else:
                q, k, v, g = a_ref[:, sl] * (HD ** -0.5), b_ref[:, sl], c_ref[:, sl], g_all[:, sl]
            dq, dk, dv, dg, dst0 = _chunk_bwd(q, k, v, g, st_ref[0, h], do_ref[:, sl], dstate[h], fw)
            dstate[h] = dst0
            if hg:
                da_ref[:, sl] = dq * _dsilu(hq)
                db_ref[:, sl] = dv
                df = dg / f - dk
                dc_ref[:, sl] = df * (1.0 - lb) * sg * (1.0 - sg)
                dlb_ref[0:1, sl] += _colsum(df * (1.0 - sg))
            else:
                da_ref[:, sl] = dq * (HD ** -0.5)
                db_ref[:, sl] = dk
                dc_ref[:, sl] = dv
                dg_parts.append(dg)
        if not hg:
            dlogits = jnp.concatenate(dg_parts, axis=1) * (1.0 / GATE_NORM) * (1.0 - _sigmoid(logits))
            dlr_ref[...] = _dot_nt(dlogits, wgk_ref[...])
            dwgk_ref[...] += _dot_tn(lr, dlogits)
            dbias_ref[0:1, :] += _colsum(dlogits)

    def chunk_of(j):
        return _chunk_index(n_chunks - 1 - j, n_ctx_chunks, n_chunks, fw)

    def cmap(blk, width=HW):
        return pl.BlockSpec((CHUNK, width), lambda j: (chunk_of(j), blk))

    fixed = lambda j: (0, 0)
    st_spec = pl.BlockSpec((1, NH, HD, HD), lambda j: (chunk_of(j), 0, 0, 0))
    big = jax.ShapeDtypeStruct((rows, HW), F32)
    if hg:
        in_specs = [cmap(cols[0]), cmap(cols[1]), cmap(cols[2]), pl.BlockSpec((4, HW), fixed), st_spec, cmap(0)]
        args = (p, p, p, side, states, d_o)
        out_shape = [big, big, big, jax.ShapeDtypeStruct((8, HW), F32)]
        out_specs = [cmap(0), cmap(0), cmap(0), pl.BlockSpec((8, HW), fixed)]
    else:
        in_specs = [cmap(cols[0]), cmap(cols[1]), cmap(cols[2]), cmap(OFF_LR // 128, 128),
                    pl.BlockSpec((128, HW), fixed), pl.BlockSpec((1, HW), fixed), st_spec, cmap(0)]
        args = (p, p, p, p, side[0], side[1], states, d_o)
        out_shape = [big, big, big, jax.ShapeDtypeStruct((rows, 128), F32), jax.ShapeDtypeStruct((128, HW), F32),
                     jax.ShapeDtypeStruct((8, HW), F32)]
        out_specs = [cmap(0), cmap(0), cmap(0), cmap(0, 128), pl.BlockSpec((128, HW), fixed),
                     pl.BlockSpec((8, HW), fixed)]
    return pl.pallas_call(
        body, name=name, grid=(n_chunks,), out_shape=out_shape, in_specs=in_specs, out_specs=out_specs,
        scratch_shapes=[pltpu.VMEM((NH, HD, HD), F32)],
        compiler_params=_cparams(dimension_semantics=("arbitrary",)),
    )(*args)


SMALL_ROWS = 56
ROWS_MOD_X = (0, 1, 8, 16, 17, 18)
ROWS_MOD_C = (2, 3)
ROW_PRE1, ROW_POST1, ROW_ONORM, ROW_PRE2, ROW_POST2, ROW_LB, ROW_BGK, ROW_WGK = 4, 9, 10, 19, 20, 24, 32, 40


def _reduce_small(gathered, lb_full, name):
    _, _, d = gathered.shape

    def body(g_ref, lb_ref, sum_ref, dmod_ref, dbmod_ref, dlb_ref):
        total = g_ref[0]
        for b in range(1, N_DEV):
            total = total + g_ref[b]
        sum_ref[...] = total
        dmod_ref[...] = jnp.zeros_like(dmod_ref)
        for m in range(N_MOD):
            col = slice(m * d, (m + 1) * d)
            acc = jnp.zeros((1, d), F32)
            for b in range(N_DEV):
                row = g_ref[b, ROWS_MOD_X[m]:ROWS_MOD_X[m] + 1, :]
                dmod_ref[b:b + 1, col] = row
                acc = acc + row
            if m < 2:
                ctx_row = total[ROWS_MOD_C[m]:ROWS_MOD_C[m] + 1, :]
                dmod_ref[8:9, col] = ctx_row
                acc = acc + ctx_row
            dbmod_ref[:, col] = acc
        lbv = lb_ref[...]
        for dd in range(2):
            lb = _sigmoid(lbv[dd:dd + 1, :] - lbv[2 + dd:3 + dd, :])
            gl = total[ROW_LB:ROW_LB + 1, dd * HW:(dd + 1) * HW] * lb * (1.0 - lb)
            dlb_ref[dd:dd + 1, :] = gl
            dlb_ref[2 + dd:3 + dd, :] = -gl

    return pl.pallas_call(
        body, name=name,
        out_shape=[jax.ShapeDtypeStruct((SMALL_ROWS, d), F32), jax.ShapeDtypeStruct((16, N_MOD * d), F32),
                   jax.ShapeDtypeStruct((1, N_MOD * d), F32), jax.ShapeDtypeStruct((4, HW), F32)],
        in_specs=[VMEM_SPEC] * 2, out_specs=[VMEM_SPEC] * 4, compiler_params=_cparams(),
    )(gathered, lb_full)


def _c_ctx_grad(gathered, c_ctx_row, name):
    def body(g_ref, c_ref, o_ref):
        acc = g_ref[0, 0:1, :]
        for chip in range(1, N_CHIP):
            acc = acc + g_ref[2 * chip, 0:1, :]
        o_ref[...] = acc * _dsilu(c_ref[...])

    return pl.pallas_call(
        body, name=name, out_shape=jax.ShapeDtypeStruct(c_ctx_row.shape, F32),
        in_specs=[VMEM_SPEC] * 2, out_specs=VMEM_SPEC, compiler_params=_cparams(),
    )(gathered, c_ctx_row)


def _relayout_w_in(w):
    pad = jnp.zeros((w.shape[0], 128 - 2 * RANK), w.dtype)
    return jnp.concatenate([w[:, :9 * HW], w[:, 9 * HW + 2 * RANK:], w[:, 9 * HW:9 * HW + 2 * RANK], pad], axis=1)


def _w_in_grad_blocks(g_main, g_lr, n_blocks):
    lr0 = 9 * HW
    n = (g_main.shape[1] + 2 * RANK) // n_blocks

    def cols(lo, hi):
        out = []
        if lo < lr0:
            out.append(g_main[:, lo:min(hi, lr0)])
        if hi > lr0 and lo < lr0 + 2 * RANK:
            out.append(g_lr[:, max(lo, lr0) - lr0:min(hi, lr0 + 2 * RANK) - lr0])
        if hi > lr0 + 2 * RANK:
            out.append(g_main[:, max(lo, lr0 + 2 * RANK) - 2 * RANK:hi - 2 * RANK])
        return out

    return jnp.stack([jnp.concatenate(cols(j * n, (j + 1) * n), axis=1) for j in range(n_blocks)])


def _blocked(full, n_blocks):
    k, n = full.shape
    return full.reshape(k, n_blocks, n // n_blocks).transpose(1, 0, 2)


def _unblocked(blocks):
    nb, k, n = blocks.shape
    return blocks.transpose(1, 0, 2).reshape(k, nb * n)


def _sample_front(x0, ctx0, modc, modx, norm_pre1, lb_full, gla_side, w_in_r):
    ctx_len = ctx0.shape[0]
    n_ctx_tiles = ctx_len // TM
    n_ctx_chunks = ctx_len // CHUNK
    z = jnp.concatenate([ctx0, x0], axis=0)
    h1, p = _in_projection(z, modc, modx, norm_pre1, w_in_r, n_ctx_tiles, "in_projection")
    o_hg_fw, st_hg_fw = _scan_fwd(p, lb_full, n_ctx_chunks, True, "hg", "scan_hg_fw")
    o_hg_bw, st_hg_bw = _scan_fwd(p, lb_full, n_ctx_chunks, False, "hg", "scan_hg_bw")
    o_gla_fw, st_gla_fw = _scan_fwd(p, gla_side[0], n_ctx_chunks, True, "gla", "scan_gla_fw")
    o_gla_bw, st_gla_bw = _scan_fwd(p, gla_side[1], n_ctx_chunks, False, "gla", "scan_gla_bw")
    return dict(z=z, h1=h1, p=p, o_list=[o_hg_fw, o_hg_bw, o_gla_fw, o_gla_bw],
                states=[st_hg_fw, st_hg_bw, st_gla_fw, st_gla_bw])


def _sample_back(reduce, front, x0, ctx0, target0, modc, modx, norm_pre1, norms, onorms, lb_full, gla_side, w_in_r,
                 wbh, wbg, wout, wg, wu, wd):
    seq, d = x0.shape
    ctx_len = ctx0.shape[0]
    n_ctx_tiles = ctx_len // TM
    n_tiles = (ctx_len + seq) // TM
    n_ctx_chunks = ctx_len // CHUNK
    z, h1, p, o_list = front["z"], front["h1"], front["p"], front["o_list"]
    st_hg_fw, st_hg_bw, st_gla_fw, st_gla_bw = front["states"]
    z2, y1, merged, og_hg, og_gla = _mixer_tail_fwd(x0, p, o_list, modx, norms, onorms, wbh, wbg, wout, n_ctx_tiles,
                                                    "mixer_tail")
    loss_part, dz2, h2, a_act, du, dv, dy2, stat_ffn = _ffn_fwd_bwd(z2, modx, norms, wg, wu, wd, target0, "ffn")
    dff = wg.shape[1]
    tn_ff = dff // N_CHIP if (dff // N_CHIP) % 128 == 0 else 256
    tok = reduce("ffn", [_weight_grad(h2, du, "grad_w_ff_gate", tn=tn_ff),
                         _weight_grad(h2, dv, "grad_w_ff_up", tn=tn_ff),
                         _weight_grad(a_act, dy2, "grad_w_ff_down", tk=dff // 2)])

    (d_ohg, d_ogla, d_hgate, d_ggate, d_ghg, d_ggla, dy1, db_hg, db_gla, stat_mix) = _mixer_tail_bwd(
        x0, p, o_list, dz2, y1, modx + tok, norms, onorms, wbh, wbg, wout, n_ctx_tiles, n_tiles, "mixer_tail_bwd")
    tok = reduce("mix", [_weight_grad(og_hg, db_hg, "grad_w_br_hg"), _weight_grad(og_gla, db_gla, "grad_w_br_gla"),
                         _weight_grad(merged, dy1, "grad_w_out")])
    lb_b = lb_full + tok
    gla_b = [(wgk, bias + tok) for wgk, bias in gla_side]
    dhq_f, dhi_f, dhf_f, dlb_f = _scan_bwd(p, lb_b, st_hg_fw, d_ohg, n_ctx_chunks, True, "hg", "scan_hg_fw_bwd")
    dhq_b, dhi_b, dhf_b, dlb_b = _scan_bwd(p, lb_b, st_hg_bw, d_ohg, n_ctx_chunks, False, "hg", "scan_hg_bw_bwd")
    dgq_f, dgk_f, dgv_f, dlr_f, dwgk_f, dbgk_f = _scan_bwd(p, gla_b[0], st_gla_fw, d_ogla, n_ctx_chunks, True, "gla",
                                                           "scan_gla_fw_bwd")
    dgq_b, dgk_b, dgv_b, dlr_b, dwgk_b, dbgk_b = _scan_bwd(p, gla_b[1], st_gla_bw, d_ogla, n_ctx_chunks, False, "gla",
                                                           "scan_gla_bw_bwd")
    pieces = [dhq_f, dhq_b, dhi_f, dhi_b, dhf_f, dhf_b, d_hgate, dgq_f, dgq_b, dgk_f, dgk_b, dgv_f, dgv_b, d_ggate,
              d_ghg, d_ggla, dlr_f, dlr_b]
    dp, grad_x, stat_in = _in_projection_bwd(z, dz2, modc, modx, norm_pre1, w_in_r, pieces, n_ctx_tiles,
                                             "in_projection_bwd")

    reduce("small", dict(stat_in=stat_in, stat_mix=stat_mix, stat_ffn=stat_ffn, dlb=(dlb_f, dlb_b),
                         dwgk=(dwgk_f, dwgk_b), dbgk=(dbgk_f, dbgk_b)))
    g_in_main = _weight_grad_cols(h1, dp, OFF_LR, "grad_w_in_main")
    g_in_lr = _weight_grad(h1, dp, "grad_w_in_lr", col_block=(128, OFF_LR // 128))
    reduce("in", [_w_in_grad_blocks(g_in_main, g_in_lr, N_CHIP)])
    return dict(
        loss_part=loss_part, grad_x=grad_x, stat_in=stat_in, stat_mix=stat_mix, stat_ffn=stat_ffn,
        dlb=(dlb_f, dlb_b), dwgk=(dwgk_f, dwgk_b), dbgk=(dbgk_f, dbgk_b))


def kernel(x, c, ctx, c_ctx, w_mod, b_mod, norm_pre1, norm_post1, norm_pre2, norm_post2, w_in, hg_lb, hg_onorm, gla_w_gk, gla_b_gk, gla_onorm, w_br_hg, w_br_gla, w_out, w_ff_gate, w_ff_up, w_ff_down, loss_target, m_c_ctx, m_w_mod, m_b_mod, m_norm_pre1, m_norm_post1, m_norm_pre2, m_norm_post2, m_w_in, m_hg_lb, m_hg_onorm, m_gla_w_gk, m_gla_b_gk, m_gla_onorm, m_w_br_hg, m_w_br_gla, m_w_out, m_w_ff_gate, m_w_ff_up, m_w_ff_down, v_c_ctx, v_w_mod, v_b_mod, v_norm_pre1, v_norm_post1, v_norm_pre2, v_norm_post2, v_w_in, v_hg_lb, v_hg_onorm, v_gla_w_gk, v_gla_b_gk, v_gla_onorm, v_w_br_hg, v_w_br_gla, v_w_out, v_w_ff_gate, v_w_ff_up, v_w_ff_down):
    seq, d = x.shape[1], x.shape[2]
    ctx_len = ctx.shape[1]
    assert seq % TM == 0 and ctx_len % TM == 0 and d == 2 * HW
    ax, ay, ac = lax.axis_index("x"), lax.axis_index("y"), lax.axis_index("c")
    chip = 2 * ax + ay
    dev = 2 * chip + ac
    c_arr = jnp.reshape(ac, (1,)).astype(jnp.int32)

    nc = d // 128
    pad8 = lambda a: jnp.pad(a, ((0, -a.shape[0] % 8), (0, 0)))
    small1 = jnp.concatenate([c.reshape(nc, 128), pad8(hg_lb.reshape(4, 128)), gla_w_gk.reshape(2 * RANK, 128),
                              pad8(gla_b_gk.reshape(2, 128))], axis=0)
    got1 = _allgather8(small1, "gather_small_params")
    c_all = got1[:, :nc, :].reshape(N_DEV, d)
    per_chip = got1[0::2]
    lb_full = per_chip[:, nc:nc + 4, :].transpose(1, 0, 2).reshape(4, HW)
    wgk_full = per_chip[:, nc + 8:nc + 8 + 2 * RANK, :].transpose(1, 0, 2).reshape(2, RANK, HW)
    bgk_full = per_chip[:, nc + 8 + 2 * RANK:nc + 10 + 2 * RANK, :].transpose(1, 0, 2).reshape(2, HW)
    wgk_pad = [jnp.zeros((128, HW), F32).at[dd * RANK:(dd + 1) * RANK].set(wgk_full[dd]) for dd in range(2)]
    bgk = [bgk_full[dd:dd + 1] for dd in range(2)]

    n_mod_cols = w_mod.shape[2]
    cond = jnp.concatenate([c_all, pad8(c_ctx.reshape(1, d))], axis=0)
    b_cols = lax.dynamic_slice(b_mod, (0, chip * n_mod_cols), (1, n_mod_cols))
    mod_part = _mod_forward(cond, w_mod[0], b_cols, "mod_forward")
    mod_got = _allgather8(mod_part, "gather_mod")
    mod_all = mod_got[0::2].transpose(1, 0, 2).reshape(16, N_CHIP * n_mod_cols)
    modx = pad8(lax.dynamic_slice(mod_all, (dev, 0), (1, N_MOD * d)).reshape(N_MOD, d))
    modc = pad8(mod_all[8].reshape(N_MOD, d))

    chip_arr = jnp.reshape(chip, (1,)).astype(jnp.int32)
    blocks = [_cast_into_blocks(chip_arr, w_[0], "cast_" + nm) for w_, nm in (
        (w_in, "w_in"), (w_br_hg, "w_br_hg"), (w_br_gla, "w_br_gla"), (w_out, "w_out"), (w_ff_gate, "w_ff_gate"),
        (w_ff_up, "w_ff_up"), (w_ff_down, "w_ff_down"))]
    gathered_in = _gather_blocks(blocks[:1], "gather_w_in")
    sems, lands, token = _blocks_start(blocks[1:], "gather_rest_start")
    w_in_r = _relayout_w_in(_unblocked(gathered_in[0]))

    norms = jnp.concatenate([norm_pre1, norm_post1, norm_pre2, norm_post2, jnp.zeros((4, d), F32)], axis=0)
    onorms = jnp.zeros((8, d), F32).at[0, :HD].set(hg_onorm[0]).at[1, :HD].set(gla_onorm[0])
    gla_side = [(wgk_pad[dd], bgk[dd]) for dd in range(2)]
    modx = modx + token[0, 0]
    front = _sample_front(x[0], ctx[0], modc, modx, norm_pre1, lb_full, gla_side, w_in_r)
    lands = _blocks_wait(sems, lands, front["o_list"][3], "gather_rest_wait")
    gathered = _blocks_finish(lands, "gather_rest_finish")
    wbh, wbg = _unblocked(gathered[0]), _unblocked(gathered[1])
    wout = gathered[2].reshape(d, d)
    wg, wu = _unblocked(gathered[3]), _unblocked(gathered[4])
    wd = gathered[5].reshape(wg.shape[1], d)
    dff = wg.shape[1]
    groups = {"ffn": ["w_ff_gate", "w_ff_up", "w_ff_down"], "mix": ["w_br_hg", "w_br_gla", "w_out"], "in": ["w_in"]}
    row_sharded = {"w_out": d // N_CHIP, "w_ff_down": dff // N_CHIP}
    in_flight = {}

    small = {}

    def reduce_small(stats):
        small2 = jnp.concatenate([
            stats["stat_in"], stats["stat_mix"], stats["stat_ffn"],
            jnp.concatenate(stats["dlb"], axis=1), jnp.concatenate(stats["dbgk"], axis=1),
            jnp.concatenate([stats["dwgk"][0][0:RANK], stats["dwgk"][1][RANK:2 * RANK]], axis=1)], axis=0)
        assert small2.shape[0] == SMALL_ROWS
        got2 = _allgather8(small2, "gather_small_grads")
        total, dmod_all, g_b_mod, g_lb_full = _reduce_small(got2, lb_full, "reduce_small")
        dmod_cols = lax.dynamic_slice(dmod_all, (0, chip * n_mod_cols), (16, n_mod_cols))
        g_w_mod, cctx_part = _mod_backward(cond, w_mod[0], dmod_cols, "mod_backward")
        got3 = _allgather8(cctx_part, "gather_c_ctx_grad")
        g_c_ctx = _c_ctx_grad(got3, c_ctx.reshape(1, d), "c_ctx_grad")
        small.update(total=total, g_b_mod=g_b_mod, g_lb_full=g_lb_full, g_w_mod=g_w_mod, g_c_ctx=g_c_ctx)

    def reduce(group, grads):
        if group == "small":
            return reduce_small(grads)
        nms = groups[group]
        full = [g if g.ndim == 3 else g.reshape(N_CHIP, row_sharded[nm], d) if nm in row_sharded
                else _blocked(g, N_CHIP) for g, nm in zip(grads, nms)]
        from_sibling = _send_other_half(full, "grads_to_sibling_" + group)
        pairs = [_pair_sum(c_arr, f, r_, "pair_sum_" + nm) for f, r_, nm in zip(full, from_sibling, nms)]
        after = [small["g_c_ctx"], small["total"]] if group == "in" else []
        sems_, pairs, lands_, token_ = _scatter_start(pairs, "grads_to_owner_start_" + group, after)
        in_flight[group] = (sems_, pairs, lands_, token_)
        return token_[0, 0]

    r = _sample_back(reduce, front, x[0], ctx[0], loss_target[0], modc, modx, norm_pre1, norms, onorms, lb_full,
                     gla_side, w_in_r, wbh, wbg, wout, wg, wu, wd)
    loss_part, grad_x, stat_in, stat_mix, stat_ffn = (r[k] for k in ("loss_part", "grad_x", "stat_in", "stat_mix",
                                                                     "stat_ffn"))
    (dlb_f, dlb_b), (dwgk_f, dwgk_b), (dbgk_f, dbgk_b) = r["dlb"], r["dwgk"], r["dbgk"]

    weights = dict(w_in=(w_in, m_w_in, v_w_in), w_br_hg=(w_br_hg, m_w_br_hg, v_w_br_hg),
                   w_br_gla=(w_br_gla, m_w_br_gla, v_w_br_gla), w_out=(w_out, m_w_out, v_w_out),
                   w_ff_gate=(w_ff_gate, m_w_ff_gate, v_w_ff_gate), w_ff_up=(w_ff_up, m_w_ff_up, v_w_ff_up),
                   w_ff_down=(w_ff_down, m_w_ff_down, v_w_ff_down))
    names = ["w_in", "w_br_hg", "w_br_gla", "w_out", "w_ff_gate", "w_ff_up", "w_ff_down"]
    big = {}

    def finish(group, after):
        sems_, pairs, lands_, _ = in_flight[group]
        pairs, lands_ = _scatter_wait(sems_, pairs, lands_, after, "grads_to_owner_wait_" + group)
        own_half = [_sum_owner(chip_arr, pr, g, "chip_sum_" + nm) for pr, g, nm in zip(pairs, lands_, groups[group])]
        other_half = _swap_with_sibling(own_half, "halves_to_sibling_" + group)
        for nm, own, oth in zip(groups[group], own_half, other_half):
            w_, m_, v_ = weights[nm]
            res = _adamw_halves(c_arr, own, oth, w_[0], m_[0], v_[0], "adamw_" + nm)
            big[nm] = [r_[None] for r_ in res]
        return big[groups[group][-1]][1]

    token_in = in_flight["in"][3]
    done_ffn = finish("ffn", [token_in])
    done_mix = finish("mix", [done_ffn])

    total, g_b_mod, g_lb_full, g_w_mod, g_c_ctx = (small[k] for k in ("total", "g_b_mod", "g_lb_full", "g_w_mod",
                                                                      "g_c_ctx"))
    g_pre1, g_post1, g_pre2, g_post2 = (total[r_:r_ + 1] for r_ in (ROW_PRE1, ROW_POST1, ROW_PRE2, ROW_POST2))
    g_hg_on, g_gla_on = total[ROW_ONORM:ROW_ONORM + 1, 0:HD], total[ROW_ONORM:ROW_ONORM + 1, HD:2 * HD]
    n_lb = hg_lb.shape[2]
    g_hg_lb = lax.dynamic_slice(g_lb_full, (0, chip * n_lb), (4, n_lb))
    g_bgk = lax.dynamic_slice(total[ROW_BGK:ROW_BGK + 1].reshape(2, HW), (0, chip * n_lb), (2, n_lb))
    g_wgk_full = total[ROW_WGK:ROW_WGK + RANK].reshape(RANK, 2, HW).transpose(1, 0, 2).reshape(2 * RANK, HW)
    g_wgk = lax.dynamic_slice(g_wgk_full, (0, chip * n_lb), (2 * RANK, n_lb))

    small_items = [
        (g_c_ctx, c_ctx.reshape(1, d), m_c_ctx.reshape(1, d), v_c_ctx.reshape(1, d)),
        (g_b_mod, b_mod, m_b_mod, v_b_mod),
        (g_pre1, norm_pre1, m_norm_pre1, v_norm_pre1),
        (g_post1, norm_post1, m_norm_post1, v_norm_post1),
        (g_pre2, norm_pre2, m_norm_pre2, v_norm_pre2),
        (g_post2, norm_post2, m_norm_post2, v_norm_post2),
        (g_hg_lb, hg_lb.reshape(4, n_lb), m_hg_lb.reshape(4, n_lb), v_hg_lb.reshape(4, n_lb)),
        (g_hg_on, hg_onorm, m_hg_onorm, v_hg_onorm),
        (g_wgk, gla_w_gk.reshape(2 * RANK, n_lb), m_gla_w_gk.reshape(2 * RANK, n_lb), v_gla_w_gk.reshape(2 * RANK, n_lb)),
        (g_bgk, gla_b_gk.reshape(2, n_lb), m_gla_b_gk.reshape(2, n_lb), v_gla_b_gk.reshape(2, n_lb)),
        (g_gla_on, gla_onorm, m_gla_onorm, v_gla_onorm),
    ]
    small_res = _adamw_whole(small_items, "adamw_small")
    mod_res = _adamw_tiled(g_w_mod, w_mod[0], m_w_mod[0], v_w_mod[0], "adamw_w_mod")
    finish("in", [done_mix, mod_res[0], small_res[0][0]])

    loss = lax.psum(loss_part[0, 0], ("x", "y", "c"))

    shapes = dict(c_ctx=c_ctx.shape, b_mod=b_mod.shape, norm_pre1=norm_pre1.shape, norm_post1=norm_post1.shape,
                  norm_pre2=norm_pre2.shape, norm_post2=norm_post2.shape, hg_lb=hg_lb.shape, hg_onorm=hg_onorm.shape,
                  gla_w_gk=gla_w_gk.shape, gla_b_gk=gla_b_gk.shape, gla_onorm=gla_onorm.shape)
    small_names = ["c_ctx", "b_mod", "norm_pre1", "norm_post1", "norm_pre2", "norm_post2", "hg_lb", "hg_onorm",
                   "gla_w_gk", "gla_b_gk", "gla_onorm"]
    grads, deltas, new_m, new_v = {}, {}, {}, {}
    for nm, item, res in zip(small_names, small_items, small_res):
        grads[nm] = item[0].reshape(shapes[nm])
        deltas[nm], new_m[nm], new_v[nm] = (r.reshape(shapes[nm]) for r in res)
    grads["w_mod"] = g_w_mod[None]
    deltas["w_mod"], new_m["w_mod"], new_v["w_mod"] = (r[None] for r in mod_res)
    for nm in names:
        grads[nm], deltas[nm], new_m[nm], new_v[nm] = big[nm]
    order = ["c_ctx", "w_mod", "b_mod", "norm_pre1", "norm_post1", "norm_pre2", "norm_post2", "w_in", "hg_lb",
             "hg_onorm", "gla_w_gk", "gla_b_gk", "gla_onorm", "w_br_hg", "w_br_gla", "w_out", "w_ff_gate", "w_ff_up",
             "w_ff_down"]
    return (loss, grad_x[None], *[grads[n] for n in order], *[deltas[n] for n in order],
            *[new_m[n] for n in order], *[new_v[n] for n in order])


def _weight_grad_cols(xs, dy, n_cols, name, tn=512):
    rows = dy.shape[0]
    k = tk = xs.shape[1]

    return pl.pallas_call(
        functools.partial(_transposed_lhs_matmul), name=name, grid=(k // tk, n_cols // tn),
        out_shape=jax.ShapeDtypeStruct((k, n_cols), F32),
        in_specs=[pl.BlockSpec((rows, tk), lambda i, j: (0, i)), pl.BlockSpec((rows, tn), lambda i, j: (0, j))],
        out_specs=pl.BlockSpec((tk, tn), lambda i, j: (i, j)),
        scratch_shapes=[pltpu.VMEM((tk, rows), BF16)],
        compiler_params=_cparams(dimension_semantics=("parallel", "arbitrary")),
    )(xs, dy)
```

```python
import functools

import jax
import jax.numpy as jnp
from jax import lax
from jax.experimental import pallas as pl
from jax.experimental.pallas import tpu as pltpu

F32 = jnp.float32
BF16 = jnp.bfloat16
HIGHEST = lax.Precision.HIGHEST
MESH = pl.DeviceIdType.MESH

EPS = 1e-6
CHUNK = 64
SUB = 16
NSUB = CHUNK // SUB
NH = 4
HD = 128
HW = NH * HD
RANK = 16
GATE_NORM = 16.0
N_MOD = 6
TM = 256
TM_FFN = 128
N_DEV = 8
N_CHIP = 4
VMEM_LIMIT = 56 * 1024 * 1024

ADAM_LR = 0.001
ADAM_B1 = 0.9
ADAM_B2 = 0.999
ADAM_EPS = 1e-08
ADAM_WD = 0.01
ADAM_STEP = 10

VMEM_SPEC = pl.BlockSpec(memory_space=pltpu.VMEM)
ANY_SPEC = pl.BlockSpec(memory_space=pl.ANY)
HBM_SPEC = pl.BlockSpec(memory_space=pltpu.HBM)
SEM_SPEC = pl.BlockSpec(memory_space=pltpu.SEMAPHORE)
EFFECT = pltpu.SideEffectType.DATAFLOW_SIDE_EFFECTING


def _cparams(**kw):
    return pltpu.CompilerParams(vmem_limit_bytes=VMEM_LIMIT, **kw)


def _dot(a, b):
    return jnp.dot(a.astype(BF16), b.astype(BF16), preferred_element_type=F32)


def _dot_nt(a, b):
    return lax.dot_general(a.astype(BF16), b.astype(BF16), (((1,), (1,)), ((), ())), preferred_element_type=F32)


def _dot_tn(a, b):
    return lax.dot_general(a.astype(BF16), b.astype(BF16), (((0,), (0,)), ((), ())), preferred_element_type=F32)


def _sigmoid(x):
    return 1.0 / (1.0 + jnp.exp(-x))


def _silu(x):
    return x * _sigmoid(x)


def _dsilu(x):
    s = _sigmoid(x)
    return s * (1.0 + x * (1.0 - s))


def _log_sigmoid(x):
    return jnp.minimum(x, 0.0) - jnp.log(1.0 + jnp.exp(-jnp.abs(x)))


def _colsum(a):
    return jnp.sum(a, axis=0, keepdims=True)


def _rms(a):
    r = lax.rsqrt(jnp.mean(a * a, axis=-1, keepdims=True) + EPS)
    return a * r, r


def _rms_bwd(dn, n, r):
    return r * (dn - n * jnp.mean(dn * n, axis=-1, keepdims=True))


def _place():
    x, y, c = lax.axis_index("x"), lax.axis_index("y"), lax.axis_index("c")
    chips = [(1 - x, y), (x, 1 - y), (1 - x, 1 - y)]
    return x, y, c, chips


def _allgather8(v, name):
    rows, cols = v.shape

    def body(x_ref, out_ref, send_sems, recv_sems, local_sem):
        x, y, c, chips = _place()
        me, sibling = (x, y, c), (x, y, 1 - c)

        def blk(px, py, pc):
            return out_ref.at[4 * px + 2 * py + pc]

        def copy(k, block, to, src=None):
            return pltpu.make_async_remote_copy(
                src_ref=blk(*block) if src is None else src, dst_ref=blk(*block),
                send_sem=send_sems.at[k], recv_sem=recv_sems.at[k], device_id=to, device_id_type=MESH)

        mine = pltpu.make_async_copy(x_ref, blk(*me), local_sem)
        mine.start()
        first = [copy(0, me, sibling, src=x_ref)]
        first += [copy(1 + j, me, (*chip, c), src=x_ref) for j, chip in enumerate(chips)]
        for cp in first:
            cp.start()
        passed = [copy(4 + j, (*chip, c), sibling) for j, chip in enumerate(chips)]
        for j, chip in enumerate(chips):
            copy(1 + j, (*chip, c), me).wait_recv()
            passed[j].start()
        copy(0, sibling, me).wait_recv()
        for j, chip in enumerate(chips):
            copy(4 + j, (*chip, 1 - c), me).wait_recv()
        for cp in first + passed:
            cp.wait_send()
        mine.wait()

    return pl.pallas_call(
        body, name=name,
        out_shape=jax.ShapeDtypeStruct((N_DEV, rows, cols), v.dtype),
        in_specs=[VMEM_SPEC], out_specs=VMEM_SPEC,
        scratch_shapes=[pltpu.SemaphoreType.DMA((7,)), pltpu.SemaphoreType.DMA((7,)), pltpu.SemaphoreType.DMA],
    )(v)


def _cast_into_blocks(chip_arr, w, name):
    rows, cols = w.shape
    tr = _row_tile(rows, 16, 256)

    def body(chip_ref, w_ref, o_ref):
        o_ref[0] = w_ref[...].astype(BF16)

    return pl.pallas_call(
        body, name=name,
        grid_spec=pltpu.PrefetchScalarGridSpec(
            num_scalar_prefetch=1, grid=(rows // tr,),
            in_specs=[pl.BlockSpec((tr, cols), lambda i, chip_ref: (i, 0))],
            out_specs=pl.BlockSpec((1, tr, cols), lambda i, chip_ref: (chip_ref[0], i, 0))),
        out_shape=jax.ShapeDtypeStruct((N_CHIP, rows, cols), BF16),
        compiler_params=_cparams(dimension_semantics=("parallel",)),
    )(chip_arr, w)


def _halved_by_rows(shape):
    return (shape[1] // 2) % 16 == 0


def _half_of(ref, pc, block=None):
    lead = slice(None) if block is None else block
    if _halved_by_rows(ref.shape):
        h = ref.shape[1] // 2
        return ref.at[lead, pl.ds(pl.multiple_of(pc * h, 16), h), :]
    h = ref.shape[2] // 2
    return ref.at[lead, :, pl.ds(pl.multiple_of(pc * h, 128), h)]


def _half_shape(shape):
    return (shape[0], shape[1] // 2, shape[2]) if _halved_by_rows(shape) else (shape[0], shape[1], shape[2] // 2)


def _half_rows(ref, chip_id, pc):
    return _half_of(ref, pc, chip_id)


def _gather_blocks(lands, name):
    n = len(lands)

    def body(*refs):
        outs = refs[n:2 * n]
        send_sems, recv_sems = refs[2 * n:]
        x, y, c, chips = _place()
        me_chip = 2 * x + y
        sibling = (x, y, 1 - c)

        def copy(k, j, chip_id, pc, to):
            return pltpu.make_async_remote_copy(
                src_ref=_half_rows(outs[k], chip_id, pc), dst_ref=_half_rows(outs[k], chip_id, pc),
                send_sem=send_sems.at[k, j], recv_sem=recv_sems.at[k, j], device_id=to, device_id_type=MESH)

        started = []
        for k in range(n):
            for j, chip in enumerate(chips):
                cp = copy(k, j, me_chip, c, (*chip, c))
                cp.start()
                started.append(cp)
        for k in range(n):
            for j, (px, py) in enumerate(chips):
                copy(k, j, 2 * px + py, c, sibling).wait_recv()
                cp = copy(k, 3 + j, 2 * px + py, c, sibling)
                cp.start()
                started.append(cp)
        for k in range(n):
            for j, (px, py) in enumerate(chips):
                copy(k, 3 + j, 2 * px + py, 1 - c, sibling).wait_recv()
        for cp in started:
            cp.wait_send()

    return pl.pallas_call(
        body, name=name,
        out_shape=[jax.ShapeDtypeStruct(l.shape, l.dtype) for l in lands],
        in_specs=[ANY_SPEC] * n, out_specs=[ANY_SPEC] * n,
        input_output_aliases={i: i for i in range(n)},
        scratch_shapes=[pltpu.SemaphoreType.DMA((n, 6)), pltpu.SemaphoreType.DMA((n, 6))],
    )(*lands)


def _hbm(a):
    return pltpu.with_memory_space_constraint(a, pltpu.HBM)


def _blocks_start(lands, name):
    n = len(lands)
    n_sem = 3 * n

    def body(*refs):
        lnd = refs[:n]
        send_sems, recv_sems = refs[n:n + n_sem], refs[n + n_sem:n + 2 * n_sem]
        token = refs[-1]
        x, y, c, chips = _place()
        me_chip = 2 * x + y
        for k in range(n):
            for j, chip in enumerate(chips):
                pltpu.make_async_remote_copy(
                    src_ref=_half_rows(lnd[k], me_chip, c), dst_ref=_half_rows(lnd[k], me_chip, c),
                    send_sem=send_sems[3 * k + j], recv_sem=recv_sems[3 * k + j],
                    device_id=(*chip, c), device_id_type=MESH).start()
        token[...] = jnp.zeros_like(token)

    out = pl.pallas_call(
        body, name=name,
        out_shape=(*[pltpu.SemaphoreType.DMA(())] * (2 * n_sem),
                   *[pltpu.HBM(l.shape, l.dtype) for l in lands],
                   jax.ShapeDtypeStruct((8, 128), F32)),
        in_specs=[HBM_SPEC] * n,
        out_specs=(*[SEM_SPEC] * (2 * n_sem), *[HBM_SPEC] * n, VMEM_SPEC),
        input_output_aliases={i: 2 * n_sem + i for i in range(n)},
        compiler_params=pltpu.CompilerParams(has_side_effects=EFFECT),
    )(*[_hbm(l) for l in lands])
    return list(out[:2 * n_sem]), list(out[2 * n_sem:2 * n_sem + n]), out[-1]


def _blocks_wait(sems, lands, after, name):
    n = len(lands)
    n_sem = 3 * n

    def body(*refs):
        lnd = refs[:n]
        s_sems, r_sems = refs[n:n + n_sem], refs[n + n_sem:n + 2 * n_sem]
        x, y, c, chips = _place()
        me_chip = 2 * x + y
        for k in range(n):
            for j, (px, py) in enumerate(chips):
                cp = pltpu.make_async_remote_copy(
                    src_ref=_half_rows(lnd[k], me_chip, c), dst_ref=_half_rows(lnd[k], 2 * px + py, c),
                    send_sem=s_sems[3 * k + j], recv_sem=r_sems[3 * k + j],
                    device_id=(px, py, c), device_id_type=MESH)
                cp.wait_send()
                cp.wait_recv()

    out = pl.pallas_call(
        body, name=name,
        out_shape=tuple(pltpu.HBM(l.shape, l.dtype) for l in lands),
        in_specs=[HBM_SPEC] * n + [SEM_SPEC] * (2 * n_sem) + [ANY_SPEC],
        out_specs=[HBM_SPEC] * n,
        input_output_aliases={i: i for i in range(n)},
        compiler_params=pltpu.CompilerParams(has_side_effects=EFFECT),
    )(*lands, *sems, after)
    return list(out)


def _blocks_finish(lands, name):
    n = len(lands)

    def body(*refs):
        lnd = refs[n:2 * n]
        send_sems, recv_sems = refs[2 * n:]
        x, y, c, chips = _place()
        sibling = (x, y, 1 - c)

        def copy(k, j, chip_id, pc):
            return pltpu.make_async_remote_copy(
                src_ref=_half_rows(lnd[k], chip_id, pc), dst_ref=_half_rows(lnd[k], chip_id, pc),
                send_sem=send_sems.at[k, j], recv_sem=recv_sems.at[k, j], device_id=sibling, device_id_type=MESH)

        started = []
        for k in range(n):
            for j, (px, py) in enumerate(chips):
                cp = copy(k, j, 2 * px + py, c)
                cp.start()
                started.append(cp)
        for k in range(n):
            for j, (px, py) in enumerate(chips):
                copy(k, j, 2 * px + py, 1 - c).wait_recv()
        for cp in started:
            cp.wait_send()

    out = pl.pallas_call(
        body, name=name,
        out_shape=[jax.ShapeDtypeStruct(l.shape, l.dtype) for l in lands],
        in_specs=[ANY_SPEC] * n, out_specs=[ANY_SPEC] * n,
        input_output_aliases={i: i for i in range(n)},
        scratch_shapes=[pltpu.SemaphoreType.DMA((n, 3)), pltpu.SemaphoreType.DMA((n, 3))],
    )(*lands)
    return list(out)


def _gather_start(shards, name):
    n = len(shards)
    n_sem = 3 * n

    def body(*refs):
        ins, lands = refs[:n], refs[n:2 * n]
        send_sems, recv_sems = refs[2 * n:2 * n + n_sem], refs[2 * n + n_sem:2 * n + 2 * n_sem]
        token = refs[-1]
        x, y, c, chips = _place()
        me_chip = 2 * x + y
        for k in range(n):
            h = shards[k].shape[0] // 2
            rows = pl.ds(pl.multiple_of(c * h, 8), h)
            for j, chip in enumerate(chips):
                pltpu.make_async_remote_copy(
                    src_ref=ins[k].at[rows, :], dst_ref=lands[k].at[me_chip, rows, :],
                    send_sem=send_sems[3 * k + j], recv_sem=recv_sems[3 * k + j],
                    device_id=(*chip, c), device_id_type=MESH).start()
        token[...] = jnp.zeros_like(token)

    lands = [_hbm(lax.empty((N_CHIP,) + s.shape, s.dtype)) for s in shards]
    out = pl.pallas_call(
        body, name=name,
        out_shape=(*[pltpu.SemaphoreType.DMA(())] * (2 * n_sem),
                   *[pltpu.HBM(s.shape, s.dtype) for s in shards],
                   *[pltpu.HBM(l.shape, l.dtype) for l in lands],
                   jax.ShapeDtypeStruct((8, 128), F32)),
        in_specs=[HBM_SPEC] * (2 * n),
        out_specs=(*[SEM_SPEC] * (2 * n_sem), *[HBM_SPEC] * (2 * n), VMEM_SPEC),
        input_output_aliases={i: 2 * n_sem + i for i in range(2 * n)},
        compiler_params=pltpu.CompilerParams(has_side_effects=EFFECT),
    )(*[_hbm(s) for s in shards], *lands)
    sems = list(out[:2 * n_sem])
    return sems, list(out[2 * n_sem:2 * n_sem + n]), list(out[2 * n_sem + n:2 * n_sem + 2 * n]), out[-1]


def _gather_wait(sems, shards, lands, after, name):
    n = len(shards)
    n_sem = 3 * n

    def body(*refs):
        ins, lnd = refs[:n], refs[n:2 * n]
        s_sems, r_sems = refs[2 * n:2 * n + n_sem], refs[2 * n + n_sem:2 * n + 2 * n_sem]
        x, y, c, chips = _place()
        for k in range(n):
            h = shards[k].shape[0] // 2
            rows = pl.ds(pl.multiple_of(c * h, 8), h)
            for j, (px, py) in enumerate(chips):
                cp = pltpu.make_async_remote_copy(
                    src_ref=ins[k].at[rows, :], dst_ref=lnd[k].at[2 * px + py, rows, :],
                    send_sem=s_sems[3 * k + j], recv_sem=r_sems[3 * k + j],
                    device_id=(px, py, c), device_id_type=MESH)
                cp.wait_send()
                cp.wait_recv()

    out = pl.pallas_call(
        body, name=name,
        out_shape=(*[pltpu.HBM(s.shape, s.dtype) for s in shards], *[pltpu.HBM(l.shape, l.dtype) for l in lands]),
        in_specs=[HBM_SPEC] * (2 * n) + [SEM_SPEC] * (2 * n_sem) + [ANY_SPEC],
        out_specs=[HBM_SPEC] * (2 * n),
        input_output_aliases={i: i for i in range(2 * n)},
        compiler_params=pltpu.CompilerParams(has_side_effects=EFFECT),
    )(*shards, *lands, *sems, after)
    return list(out[:n]), list(out[n:])


def _gather_finish(shards, lands, name):
    n = len(shards)

    def body(*refs):
        ins, lnd = refs[:n], refs[2 * n:3 * n]
        send_sems, recv_sems, local_sems = refs[3 * n:]
        x, y, c, chips = _place()
        me_chip = 2 * x + y
        sibling = (x, y, 1 - c)

        def half(k, chip_id, pc):
            h = shards[k].shape[0] // 2
            return lnd[k].at[chip_id, pl.ds(pl.multiple_of(pc * h, 8), h), :]

        def copy(k, j, chip_id, pc):
            return pltpu.make_async_remote_copy(
                src_ref=half(k, chip_id, pc), dst_ref=half(k, chip_id, pc),
                send_sem=send_sems.at[k, j], recv_sem=recv_sems.at[k, j], device_id=sibling, device_id_type=MESH)

        locals_, started = [], []
        for k in range(n):
            cp = pltpu.make_async_copy(ins[k], lnd[k].at[me_chip], local_sems.at[k])
            cp.start()
            locals_.append(cp)
            for j, (px, py) in enumerate(chips):
                cp = copy(k, j, 2 * px + py, c)
                cp.start()
                started.append(cp)
        for k in range(n):
            for j, (px, py) in enumerate(chips):
                copy(k, j, 2 * px + py, 1 - c).wait_recv()
        for cp in started:
            cp.wait_send()
        for cp in locals_:
            cp.wait()

    out = pl.pallas_call(
        body, name=name,
        out_shape=[jax.ShapeDtypeStruct(l.shape, l.dtype) for l in lands],
        in_specs=[ANY_SPEC] * (2 * n), out_specs=[ANY_SPEC] * n,
        input_output_aliases={n + i: i for i in range(n)},
        scratch_shapes=[pltpu.SemaphoreType.DMA((n, 3)), pltpu.SemaphoreType.DMA((n, 3)),
                        pltpu.SemaphoreType.DMA((n,))],
    )(*shards, *lands)
    return list(out)


def _send_other_half(arrs, name):
    n = len(arrs)

    def body(*refs):
        ins, outs = refs[:n], refs[n:2 * n]
        send_sems, recv_sems = refs[2 * n:]
        x, y, c, _ = _place()
        cps = []
        for k in range(n):
            cp = pltpu.make_async_remote_copy(
                src_ref=_half_of(ins[k], 1 - c), dst_ref=outs[k],
                send_sem=send_sems.at[k], recv_sem=recv_sems.at[k], device_id=(x, y, 1 - c), device_id_type=MESH)
            cp.start()
            cps.append(cp)
        for cp in cps:
            cp.wait()

    return pl.pallas_call(
        body, name=name,
        out_shape=[jax.ShapeDtypeStruct(_half_shape(a.shape), a.dtype) for a in arrs],
        in_specs=[ANY_SPEC] * n, out_specs=[ANY_SPEC] * n,
        scratch_shapes=[pltpu.SemaphoreType.DMA((n,)), pltpu.SemaphoreType.DMA((n,))],
    )(*arrs)


def _blocks_to_owner(arrs, name):
    n = len(arrs)

    def body(*refs):
        ins, outs = refs[:n], refs[n:2 * n]
        send_sems, recv_sems, local_sems = refs[2 * n:]
        x, y, c, chips = _place()
        me_chip = 2 * x + y
        locals_, started = [], []
        for k in range(n):
            cp = pltpu.make_async_copy(ins[k].at[me_chip], outs[k].at[me_chip], local_sems.at[k])
            cp.start()
            locals_.append(cp)

        def copy(k, j, src_block, dst_slot, to):
            return pltpu.make_async_remote_copy(
                src_ref=ins[k].at[src_block], dst_ref=outs[k].at[dst_slot],
                send_sem=send_sems.at[k, j], recv_sem=recv_sems.at[k, j], device_id=to, device_id_type=MESH)

        for k in range(n):
            for j, (px, py) in enumerate(chips):
                cp = copy(k, j, 2 * px + py, me_chip, (px, py, c))
                cp.start()
                started.append(cp)
        for k in range(n):
            for j, (px, py) in enumerate(chips):
                copy(k, j, me_chip, 2 * px + py, (px, py, c)).wait_recv()
        for cp in started:
            cp.wait_send()
        for cp in locals_:
            cp.wait()

    return pl.pallas_call(
        body, name=name,
        out_shape=[jax.ShapeDtypeStruct(a.shape, a.dtype) for a in arrs],
        in_specs=[ANY_SPEC] * n, out_specs=[ANY_SPEC] * n,
        scratch_shapes=[pltpu.SemaphoreType.DMA((n, 3)), pltpu.SemaphoreType.DMA((n, 3)),
                        pltpu.SemaphoreType.DMA((n,))],
    )(*arrs)


def _scatter_blocks(arrs, name):
    n = len(arrs)

    def body(*refs):
        ins, outs = refs[:n], refs[n:2 * n]
        send_sems, recv_sems = refs[2 * n:]
        x, y, c, chips = _place()
        me_chip = 2 * x + y

        def copy(k, j, src_block, dst_slot, to):
            return pltpu.make_async_remote_copy(
                src_ref=ins[k].at[src_block], dst_ref=outs[k].at[dst_slot],
                send_sem=send_sems.at[k, j], recv_sem=recv_sems.at[k, j], device_id=to, device_id_type=MESH)

        started = []
        for k in range(n):
            for j, (px, py) in enumerate(chips):
                cp = copy(k, j, 2 * px + py, me_chip, (px, py, c))
                cp.start()
                started.append(cp)
        for k in range(n):
            for j, (px, py) in enumerate(chips):
                copy(k, j, me_chip, 2 * px + py, (px, py, c)).wait_recv()
        for cp in started:
            cp.wait_send()

    return pl.pallas_call(
        body, name=name,
        out_shape=[jax.ShapeDtypeStruct(a.shape, a.dtype) for a in arrs],
        in_specs=[ANY_SPEC] * n, out_specs=[ANY_SPEC] * n,
        scratch_shapes=[pltpu.SemaphoreType.DMA((n, 3)), pltpu.SemaphoreType.DMA((n, 3))],
    )(*arrs)


def _scatter_start(arrs, name, after=()):
    n = len(arrs)
    n_sem = 3 * n
    first = 2 * n + len(after)

    def body(*refs):
        ins, lnd = refs[:n], refs[n:2 * n]
        send_sems, recv_sems = refs[first:first + n_sem], refs[first + n_sem:first + 2 * n_sem]
        token = refs[-1]
        x, y, c, chips = _place()
        me_chip = 2 * x + y
        for k in range(n):
            for j, (px, py) in enumerate(chips):
                pltpu.make_async_remote_copy(
                    src_ref=ins[k].at[2 * px + py], dst_ref=lnd[k].at[me_chip],
                    send_sem=send_sems[3 * k + j], recv_sem=recv_sems[3 * k + j],
                    device_id=(px, py, c), device_id_type=MESH).start()
        token[...] = jnp.zeros_like(token)

    lands = [_hbm(lax.empty(a.shape, a.dtype)) for a in arrs]
    out = pl.pallas_call(
        body, name=name,
        out_shape=(*[pltpu.SemaphoreType.DMA(())] * (2 * n_sem),
                   *[pltpu.HBM(a.shape, a.dtype) for a in arrs], *[pltpu.HBM(a.shape, a.dtype) for a in arrs],
                   jax.ShapeDtypeStruct((8, 128), F32)),
        in_specs=[HBM_SPEC] * (2 * n) + [ANY_SPEC] * len(after),
        out_specs=(*[SEM_SPEC] * (2 * n_sem), *[HBM_SPEC] * (2 * n), VMEM_SPEC),
        input_output_aliases={i: 2 * n_sem + i for i in range(2 * n)},
        compiler_params=pltpu.CompilerParams(has_side_effects=EFFECT),
    )(*[_hbm(a) for a in arrs], *lands, *after)
    base = 2 * n_sem
    return list(out[:base]), list(out[base:base + n]), list(out[base + n:base + 2 * n]), out[-1]


def _scatter_wait(sems, arrs, lands, after, name):
    n = len(arrs)
    n_sem = 3 * n

    def body(*refs):
        ins, lnd = refs[:n], refs[n:2 * n]
        s_sems, r_sems = refs[2 * n:2 * n + n_sem], refs[2 * n + n_sem:2 * n + 2 * n_sem]
        x, y, c, chips = _place()
        for k in range(n):
            for j, (px, py) in enumerate(chips):
                cp = pltpu.make_async_remote_copy(
                    src_ref=ins[k].at[2 * px + py], dst_ref=lnd[k].at[2 * px + py],
                    send_sem=s_sems[3 * k + j], recv_sem=r_sems[3 * k + j],
                    device_id=(px, py, c), device_id_type=MESH)
                cp.wait_send()
                cp.wait_recv()

    out = pl.pallas_call(
        body, name=name,
        out_shape=tuple(pltpu.HBM(a.shape, a.dtype) for a in list(arrs) + list(lands)),
        in_specs=[HBM_SPEC] * (2 * n) + [SEM_SPEC] * (2 * n_sem) + [ANY_SPEC] * len(after),
        out_specs=[HBM_SPEC] * (2 * n),
        input_output_aliases={i: i for i in range(2 * n)},
        compiler_params=pltpu.CompilerParams(has_side_effects=EFFECT),
    )(*arrs, *lands, *sems, *after)
    return list(out[:n]), list(out[n:])


def _sum_owner(chip_arr, pairs, got, name):
    nb, h, cols = got.shape
    tr = _row_tile(h, 16, 256)

    def body(chip_ref, own_ref, a_ref, b_ref, c_ref, o_ref):
        o_ref[...] = ((own_ref[0].astype(F32) + a_ref[0].astype(F32)) + b_ref[0].astype(F32)) + c_ref[0].astype(F32)

    def slot(off):
        return pl.BlockSpec((1, tr, cols), lambda i, chip_ref: ((chip_ref[0] + off) % N_CHIP, i, 0))

    return pl.pallas_call(
        body, name=name,
        grid_spec=pltpu.PrefetchScalarGridSpec(
            num_scalar_prefetch=1, grid=(h // tr,),
            in_specs=[slot(0), slot(1), slot(2), slot(3)],
            out_specs=pl.BlockSpec((tr, cols), lambda i, chip_ref: (i, 0))),
        out_shape=jax.ShapeDtypeStruct((h, cols), F32),
        compiler_params=_cparams(dimension_semantics=("parallel",)),
    )(chip_arr, pairs, got, got, got)


def _swap_with_sibling(arrs, name):
    n = len(arrs)

    def body(*refs):
        ins, outs = refs[:n], refs[n:2 * n]
        send_sems, recv_sems = refs[2 * n:]
        x, y, c, _ = _place()
        cps = []
        for k in range(n):
            cp = pltpu.make_async_remote_copy(
                src_ref=ins[k], dst_ref=outs[k], send_sem=send_sems.at[k], recv_sem=recv_sems.at[k],
                device_id=(x, y, 1 - c), device_id_type=MESH)
            cp.start()
            cps.append(cp)
        for cp in cps:
            cp.wait()

    return pl.pallas_call(
        body, name=name,
        out_shape=[jax.ShapeDtypeStruct(a.shape, a.dtype) for a in arrs],
        in_specs=[ANY_SPEC] * n, out_specs=[ANY_SPEC] * n,
        scratch_shapes=[pltpu.SemaphoreType.DMA((n,)), pltpu.SemaphoreType.DMA((n,))],
    )(*arrs)


def _row_tile(h, mult=8, cap=128):
    for t in range(cap - cap % mult, mult - 1, -mult):
        if h % t == 0:
            return t
    if mult > 8:
        return _row_tile(h, 8, cap)
    raise ValueError(h)


def _cast_bf16(a, name):
    rows, cols = a.shape
    tr = _row_tile(rows, 16, 256)

    def body(a_ref, o_ref):
        o_ref[...] = a_ref[...].astype(BF16)

    return pl.pallas_call(
        body, name=name, grid=(rows // tr,),
        out_shape=jax.ShapeDtypeStruct(a.shape, BF16),
        in_specs=[pl.BlockSpec((tr, cols), lambda i: (i, 0))],
        out_specs=pl.BlockSpec((tr, cols), lambda i: (i, 0)),
        compiler_params=_cparams(dimension_semantics=("parallel",)),
    )(a)


def _pair_sum(c_arr, full, recv, name):
    nb, rows, cols = full.shape

    def body(c_ref, f_ref, r_ref, o_ref):
        o_ref[...] = (f_ref[...] + r_ref[...]).astype(BF16)

    if _halved_by_rows(full.shape):
        h = rows // 2
        tr = _row_tile(h, 16, 256)
        steps = h // tr
        own = pl.BlockSpec((1, tr, cols), lambda b, i, c_ref: (b, c_ref[0] * steps + i, 0))
        half = pl.BlockSpec((1, tr, cols), lambda b, i, c_ref: (b, i, 0))
    else:
        steps = 1
        own = pl.BlockSpec((1, rows, cols // 2), lambda b, i, c_ref: (b, 0, c_ref[0]))
        half = pl.BlockSpec((1, rows, cols // 2), lambda b, i, c_ref: (b, 0, 0))
    return pl.pallas_call(
        body, name=name,
        grid_spec=pltpu.PrefetchScalarGridSpec(
            num_scalar_prefetch=1, grid=(nb, steps), in_specs=[own, half], out_specs=half),
        out_shape=jax.ShapeDtypeStruct(_half_shape(full.shape), BF16),
        compiler_params=_cparams(dimension_semantics=("parallel", "parallel")),
    )(c_arr, full, recv)


def _sum_chips(got, name):
    nb, h, cols = got.shape
    tr = _row_tile(h, 16, 256)

    def body(g_ref, o_ref):
        g = g_ref[...].astype(F32)
        o_ref[...] = ((g[0] + g[1]) + g[2]) + g[3]

    return pl.pallas_call(
        body, name=name, grid=(h // tr,),
        out_shape=jax.ShapeDtypeStruct((h, cols), F32),
        in_specs=[pl.BlockSpec((nb, tr, cols), lambda i: (0, i, 0))],
        out_specs=pl.BlockSpec((tr, cols), lambda i: (i, 0)),
        compiler_params=_cparams(dimension_semantics=("parallel",)),
    )(got)


def _adam_math(g, w, m, v):
    m1 = ADAM_B1 * m + (1.0 - ADAM_B1) * g
    v1 = ADAM_B2 * v + (1.0 - ADAM_B2) * (g * g)
    m_hat = m1 / (1.0 - ADAM_B1 ** ADAM_STEP)
    v_hat = v1 / (1.0 - ADAM_B2 ** ADAM_STEP)
    delta = -ADAM_LR * (m_hat / (jnp.sqrt(v_hat) + ADAM_EPS) + ADAM_WD * w)
    return delta, m1, v1


def _adamw_halves(c_arr, own, other, w, m, v, name):
    rows, cols = w.shape

    def body(c_ref, own_ref, oth_ref, w_ref, m_ref, v_ref, g_out, d_out, m_out, v_out):
        g = jnp.where(pl.program_id(0) == c_ref[0], own_ref[...], oth_ref[...])
        d, m1, v1 = _adam_math(g, w_ref[...], m_ref[...], v_ref[...])
        g_out[...] = g
        d_out[...] = d
        m_out[...] = m1
        v_out[...] = v1

    if own.shape[1] == cols:
        h = rows // 2
        tr = _row_tile(h)
        steps = h // tr
        half_spec = pl.BlockSpec((tr, cols), lambda p, i, c_ref: (i, 0))
        full_spec = pl.BlockSpec((tr, cols), lambda p, i, c_ref: (p * steps + i, 0))
    else:
        tr = _row_tile(rows)
        steps = rows // tr
        half_spec = pl.BlockSpec((tr, cols // 2), lambda p, i, c_ref: (i, 0))
        full_spec = pl.BlockSpec((tr, cols // 2), lambda p, i, c_ref: (i, p))
    return pl.pallas_call(
        body, name=name,
        grid_spec=pltpu.PrefetchScalarGridSpec(
            num_scalar_prefetch=1, grid=(2, steps),
            in_specs=[half_spec, half_spec, full_spec, full_spec, full_spec],
            out_specs=[full_spec] * 4),
        out_shape=[jax.ShapeDtypeStruct(w.shape, F32)] * 4,
        compiler_params=_cparams(dimension_semantics=("parallel", "parallel")),
    )(c_arr, own, other, w, m, v)


def _adamw_whole(items, name):
    n = len(items)

    def body(*refs):
        ins, outs = refs[:4 * n], refs[4 * n:]
        for k in range(n):
            g, w, m, v = (r[...] for r in ins[4 * k:4 * k + 4])
            d, m1, v1 = _adam_math(g, w, m, v)
            outs[3 * k][...] = d
            outs[3 * k + 1][...] = m1
            outs[3 * k + 2][...] = v1

    flat = [a for it in items for a in it]
    shapes = [jax.ShapeDtypeStruct(it[1].shape, F32) for it in items for _ in range(3)]
    out = pl.pallas_call(
        body, name=name, out_shape=shapes,
        in_specs=[VMEM_SPEC] * (4 * n), out_specs=[VMEM_SPEC] * (3 * n),
        compiler_params=_cparams(),
    )(*flat)
    return [tuple(out[3 * k:3 * k + 3]) for k in range(n)]


def _adamw_tiled(g, w, m, v, name):
    rows, cols = w.shape
    tr = _row_tile(rows)

    def body(g_ref, w_ref, m_ref, v_ref, d_out, m_out, v_out):
        d, m1, v1 = _adam_math(g_ref[...], w_ref[...], m_ref[...], v_ref[...])
        d_out[...] = d
        m_out[...] = m1
        v_out[...] = v1

    spec = pl.BlockSpec((tr, cols), lambda i: (i, 0))
    return pl.pallas_call(
        body, name=name, grid=(rows // tr,),
        out_shape=[jax.ShapeDtypeStruct(w.shape, F32)] * 3,
        in_specs=[spec] * 4, out_specs=[spec] * 3,
        compiler_params=_cparams(dimension_semantics=("parallel",)),
    )(g, w, m, v)


def _mod_forward(cond, w_mod, b_mod_cols, name):
    def body(c_ref, w_ref, b_ref, o_ref):
        o_ref[...] = _dot(_silu(c_ref[...]), w_ref[...]) + b_ref[...]

    return pl.pallas_call(
        body, name=name, out_shape=jax.ShapeDtypeStruct((cond.shape[0], w_mod.shape[1]), F32),
        in_specs=[VMEM_SPEC] * 3, out_specs=VMEM_SPEC, compiler_params=_cparams(),
    )(cond, w_mod, b_mod_cols)


def _mod_backward(cond, w_mod, dmod_cols, name):
    def body(c_ref, w_ref, d_ref, gw_ref, gc_ref):
        s = _silu(c_ref[...])
        d = d_ref[...]
        gw_ref[...] = _dot_tn(s, d)
        gc_ref[...] = _dot_nt(d[8:16, :], w_ref[...])

    return pl.pallas_call(
        body, name=name,
        out_shape=[jax.ShapeDtypeStruct(w_mod.shape, F32), jax.ShapeDtypeStruct((8, w_mod.shape[0]), F32)],
        in_specs=[VMEM_SPEC] * 3, out_specs=[VMEM_SPEC] * 2, compiler_params=_cparams(),
    )(cond, w_mod, dmod_cols)


def _col_chunks(width, step=512):
    return [(s, min(step, width - s)) for s in range(0, width, step)]


def _in_projection(z, modc, modx, pre1, w_r, n_ctx_tiles, name):
    rows, d = z.shape
    width = w_r.shape[0]

    def body(z_ref, modc_ref, modx_ref, pre_ref, w_ref, h_ref, p_ref):
        is_ctx = pl.program_id(0) < n_ctx_tiles
        n, _ = _rms(z_ref[...])
        shift = jnp.where(is_ctx, modc_ref[0:1, :], modx_ref[0:1, :])
        scale = jnp.where(is_ctx, modc_ref[1:2, :], modx_ref[1:2, :])
        h = (n * pre_ref[...] * (1.0 + scale) + shift).astype(BF16)
        h_ref[...] = h
        for s, w in _col_chunks(width):
            p_ref[:, s:s + w] = _dot_nt(h, w_ref[s:s + w, :])

    row = lambda i: (i, 0)
    fixed = lambda i: (0, 0)
    return pl.pallas_call(
        body, name=name, grid=(rows // TM,),
        out_shape=[jax.ShapeDtypeStruct((rows, d), BF16), jax.ShapeDtypeStruct((rows, width), F32)],
        in_specs=[pl.BlockSpec((TM, d), row), pl.BlockSpec((8, d), fixed), pl.BlockSpec((8, d), fixed),
                  pl.BlockSpec((1, d), fixed), VMEM_SPEC],
        out_specs=[pl.BlockSpec((TM, d), row), pl.BlockSpec((TM, width), row)],
        compiler_params=_cparams(dimension_semantics=("parallel",)),
    )(z, modc, modx, pre1, w_r)


C_HQ, C_HI, C_HF_FW, C_HF_BW, C_HGATE, C_GQ, C_GK, C_GV, C_GGATE = range(9)
OFF_GATE_HG = 9 * HW
OFF_LR = 13 * HW
P_WIDTH = OFF_LR + 128


def _head_norm_fwd(o, w):
    outs, ns, rs = [], [], []
    for h in range(NH):
        n, r = _rms(o[:, h * HD:(h + 1) * HD])
        ns.append(n)
        rs.append(r)
        outs.append(n * w)
    return jnp.concatenate(outs, axis=1), ns, rs


def _mixer_tail(z, o_hg, o_gla, p_hgate, p_ggate, p_gate_hg, p_gate_gla, hg_on, gla_on, wbh, wbg, wout):
    on_hg, n_hg, r_hg = _head_norm_fwd(o_hg, hg_on)
    on_gla, n_gla, r_gla = _head_norm_fwd(o_gla, gla_on)
    og_hg = (on_hg * _silu(p_hgate)).astype(BF16)
    og_gla = (on_gla * _silu(p_ggate)).astype(BF16)
    b_hg = jnp.dot(og_hg, wbh, preferred_element_type=F32)
    b_gla = jnp.dot(og_gla, wbg, preferred_element_type=F32)
    s_hg = _sigmoid(p_gate_hg)
    s_gla = _sigmoid(p_gate_gla)
    merged = (s_hg * b_hg + s_gla * b_gla).astype(BF16)
    y1 = jnp.dot(merged, wout, preferred_element_type=F32)
    return dict(on_hg=on_hg, n_hg=n_hg, r_hg=r_hg, on_gla=on_gla, n_gla=n_gla, r_gla=r_gla, og_hg=og_hg,
                og_gla=og_gla, b_hg=b_hg, b_gla=b_gla, s_hg=s_hg, s_gla=s_gla, merged=merged, y1=y1)


def _mixer_ffn(x_lat, p, o_list, modx, norms, onorms, w_br_hg, w_br_gla, w_out, w_gate, w_up, w_down, target,
               n_ctx_tiles, name):
    rows, d = x_lat.shape
    dff = w_gate.shape[0]
    inv_d = 1.0 / d

    def body(x_ref, ofw_hg, obw_hg, ofw_gla, obw_gla, p_hgate, p_ggate, p_ghg_a, p_ghg_b, p_ggla_a, p_ggla_b,
             modx_ref, norm_ref, on_ref, wbh_ref, wbg_ref, wout_ref, wg_ref, wu_ref, wd_ref, t_ref,
             loss_ref, dz2_ref, y1_ref, mrg_ref, oghg_ref, oggla_ref, h2_ref, a_ref, du_ref, dv_ref, dy2_ref,
             stat_ref):
        i = pl.program_id(0)
        post1, pre2, post2 = norm_ref[1:2, :], norm_ref[2:3, :], norm_ref[3:4, :]
        gate1, shift2, scale2, gate2 = modx_ref[2:3, :], modx_ref[3:4, :], modx_ref[4:5, :], modx_ref[5:6, :]
        p_gate_hg = jnp.concatenate([p_ghg_a[...], p_ghg_b[...]], axis=1)
        p_gate_gla = jnp.concatenate([p_ggla_a[...], p_ggla_b[...]], axis=1)
        t = _mixer_tail(x_ref[...], ofw_hg[...] + obw_hg[...], ofw_gla[...] + obw_gla[...], p_hgate[...],
                        p_ggate[...], p_gate_hg, p_gate_gla, on_ref[0:1, 0:HD], on_ref[1:2, 0:HD],
                        wbh_ref[...], wbg_ref[...], wout_ref[...])
        y1_ref[...] = t["y1"]
        mrg_ref[...] = t["merged"]
        oghg_ref[...] = t["og_hg"]
        oggla_ref[...] = t["og_gla"]
        n1, _ = _rms(t["y1"])
        z2 = x_ref[...] + n1 * post1 * gate1
        n2, r2 = _rms(z2)
        nw2 = n2 * pre2
        h2 = (nw2 * (1.0 + scale2) + shift2).astype(BF16)
        h2_ref[...] = h2
        u = _dot_nt(h2, wg_ref[...])
        v = _dot_nt(h2, wu_ref[...])
        su = _silu(u)
        a = (su * v).astype(BF16)
        a_ref[...] = a
        y2 = jnp.dot(a, wd_ref[...], preferred_element_type=F32)
        n3, r3 = _rms(y2)
        z3 = z2 + n3 * post2 * gate2
        err = z3 - t_ref[...]
        part = 0.5 * inv_d * jnp.sum(err * err)
        dz3 = err * inv_d
        dgate2 = _colsum(dz3 * n3 * post2)
        tt = dz3 * gate2
        dpost2 = _colsum(tt * n3)
        dy2 = _rms_bwd(tt * post2, n3, r3).astype(BF16)
        dy2_ref[...] = dy2
        da = _dot_nt(dy2, wd_ref[...])
        du = (da * v * _dsilu(u)).astype(BF16)
        dv = (da * su).astype(BF16)
        du_ref[...] = du
        dv_ref[...] = dv
        dh2 = (jnp.dot(du, wg_ref[...], preferred_element_type=F32)
               + jnp.dot(dv, wu_ref[...], preferred_element_type=F32))
        dshift2 = _colsum(dh2)
        dscale2 = _colsum(dh2 * nw2)
        dnw2 = dh2 * (1.0 + scale2)
        dpre2 = _colsum(dnw2 * n2)
        dz2_ref[...] = dz3 + _rms_bwd(dnw2 * pre2, n2, r2)

        @pl.when(i == 0)
        def _():
            stat_ref[...] = jnp.zeros_like(stat_ref)
            loss_ref[...] = jnp.zeros_like(loss_ref)

        for r, val in enumerate((dshift2, dscale2, dgate2, dpre2, dpost2)):
            stat_ref[r:r + 1, :] += val
        loss_ref[...] += part

    tm = TM_FFN
    ctx_tiles = n_ctx_tiles * (TM // tm)
    lat = lambda i: (i, 0)
    full = lambda i: (i + ctx_tiles, 0)
    fixed = lambda i: (0, 0)

    def pcol(blk):
        return pl.BlockSpec((tm, HW), lambda i: (i + ctx_tiles, blk))

    in_specs = ([pl.BlockSpec((tm, d), lat)] + [pl.BlockSpec((tm, HW), full)] * 4
                + [pcol(C_HGATE), pcol(C_GGATE), pcol(9), pcol(10), pcol(11), pcol(12)]
                + [pl.BlockSpec((8, d), fixed), pl.BlockSpec((8, d), fixed), pl.BlockSpec((8, d), fixed)]
                + [VMEM_SPEC] * 6 + [pl.BlockSpec((tm, d), lat)])
    bf = lambda w: jax.ShapeDtypeStruct((rows, w), BF16)
    out_shape = [jax.ShapeDtypeStruct((8, 128), F32), jax.ShapeDtypeStruct((rows, d), F32),
                 jax.ShapeDtypeStruct((rows, d), F32), bf(d), bf(HW), bf(HW), bf(d), bf(dff), bf(dff), bf(dff), bf(d),
                 jax.ShapeDtypeStruct((8, d), F32)]
    out_specs = [pl.BlockSpec((8, 128), fixed), pl.BlockSpec((tm, d), lat), pl.BlockSpec((tm, d), lat),
                 pl.BlockSpec((tm, d), lat), pl.BlockSpec((tm, HW), lat), pl.BlockSpec((tm, HW), lat),
                 pl.BlockSpec((tm, d), lat), pl.BlockSpec((tm, dff), lat), pl.BlockSpec((tm, dff), lat),
                 pl.BlockSpec((tm, dff), lat), pl.BlockSpec((tm, d), lat), pl.BlockSpec((8, d), fixed)]
    return pl.pallas_call(
        body, name=name, grid=(rows // tm,), out_shape=out_shape, in_specs=in_specs, out_specs=out_specs,
        compiler_params=_cparams(dimension_semantics=("arbitrary",)),
    )(x_lat, *o_list, p, p, p, p, p, p, modx, norms, onorms, w_br_hg, w_br_gla, w_out, w_gate, w_up, w_down, target)


def _mixer_tail_fwd(x_lat, p, o_list, modx, norms, onorms, w_br_hg, w_br_gla, w_out, n_ctx_tiles, name):
    rows, d = x_lat.shape

    def body(x_ref, ofw_hg, obw_hg, ofw_gla, obw_gla, p_hgate, p_ggate, p_ghg_a, p_ghg_b, p_ggla_a, p_ggla_b,
             modx_ref, norm_ref, on_ref, wbh_ref, wbg_ref, wout_ref, z2_ref, y1_ref, mrg_ref, oghg_ref, oggla_ref):
        p_gate_hg = jnp.concatenate([p_ghg_a[...], p_ghg_b[...]], axis=1)
        p_gate_gla = jnp.concatenate([p_ggla_a[...], p_ggla_b[...]], axis=1)
        t = _mixer_tail(x_ref[...], ofw_hg[...] + obw_hg[...], ofw_gla[...] + obw_gla[...], p_hgate[...],
                        p_ggate[...], p_gate_hg, p_gate_gla, on_ref[0:1, 0:HD], on_ref[1:2, 0:HD],
                        wbh_ref[...], wbg_ref[...], wout_ref[...])
        y1_ref[...] = t["y1"]
        mrg_ref[...] = t["merged"]
        oghg_ref[...] = t["og_hg"]
        oggla_ref[...] = t["og_gla"]
        n1, _ = _rms(t["y1"])
        z2_ref[...] = x_ref[...] + n1 * norm_ref[1:2, :] * modx_ref[2:3, :]

    lat = lambda i: (i, 0)
    full = lambda i: (i + n_ctx_tiles, 0)
    fixed = lambda i: (0, 0)

    def pcol(blk):
        return pl.BlockSpec((TM, HW), lambda i: (i + n_ctx_tiles, blk))

    in_specs = ([pl.BlockSpec((TM, d), lat)] + [pl.BlockSpec((TM, HW), full)] * 4
                + [pcol(C_HGATE), pcol(C_GGATE), pcol(9), pcol(10), pcol(11), pcol(12)]
                + [pl.BlockSpec((8, d), fixed)] * 3 + [VMEM_SPEC] * 3)
    bf = lambda w: jax.ShapeDtypeStruct((rows, w), BF16)
    f32 = jax.ShapeDtypeStruct((rows, d), F32)
    return pl.pallas_call(
        body, name=name, grid=(rows // TM,), out_shape=[f32, f32, bf(d), bf(HW), bf(HW)], in_specs=in_specs,
        out_specs=[pl.BlockSpec((TM, d), lat)] * 3 + [pl.BlockSpec((TM, HW), lat)] * 2,
        compiler_params=_cparams(dimension_semantics=("parallel",)),
    )(x_lat, *o_list, p, p, p, p, p, p, modx, norms, onorms, w_br_hg, w_br_gla, w_out)


def _ffn_fwd_bwd(z2, modx, norms, w_gate, w_up, w_down, target, name):
    rows, d = z2.shape
    dff = w_gate.shape[0]
    inv_d = 1.0 / d

    def body(z2_ref, modx_ref, norm_ref, wg_ref, wu_ref, wd_ref, t_ref,
             loss_ref, dz2_ref, h2_ref, a_ref, du_ref, dv_ref, dy2_ref, stat_ref):
        i = pl.program_id(0)
        pre2, post2 = norm_ref[2:3, :], norm_ref[3:4, :]
        shift2, scale2, gate2 = modx_ref[3:4, :], modx_ref[4:5, :], modx_ref[5:6, :]
        z2 = z2_ref[...]
        n2, r2 = _rms(z2)
        nw2 = n2 * pre2
        h2 = (nw2 * (1.0 + scale2) + shift2).astype(BF16)
        h2_ref[...] = h2
        u = _dot_nt(h2, wg_ref[...])
        v = _dot_nt(h2, wu_ref[...])
        su = _silu(u)
        a = (su * v).astype(BF16)
        a_ref[...] = a
        y2 = jnp.dot(a, wd_ref[...], preferred_element_type=F32)
        n3, r3 = _rms(y2)
        err = z2 + n3 * post2 * gate2 - t_ref[...]
        part = 0.5 * inv_d * jnp.sum(err * err)
        dz3 = err * inv_d
        dgate2 = _colsum(dz3 * n3 * post2)
        tt = dz3 * gate2
        dpost2 = _colsum(tt * n3)
        dy2 = _rms_bwd(tt * post2, n3, r3).astype(BF16)
        dy2_ref[...] = dy2
        da = _dot_nt(dy2, wd_ref[...])
        du = (da * v * _dsilu(u)).astype(BF16)
        dv = (da * su).astype(BF16)
        du_ref[...] = du
        dv_ref[...] = dv
        dh2 = (jnp.dot(du, wg_ref[...], preferred_element_type=F32)
               + jnp.dot(dv, wu_ref[...], preferred_element_type=F32))
        dshift2 = _colsum(dh2)
        dscale2 = _colsum(dh2 * nw2)
        dnw2 = dh2 * (1.0 + scale2)
        dpre2 = _colsum(dnw2 * n2)
        dz2_ref[...] = dz3 + _rms_bwd(dnw2 * pre2, n2, r2)

        @pl.when(i == 0)
        def _():
            stat_ref[...] = jnp.zeros_like(stat_ref)
            loss_ref[...] = jnp.zeros_like(loss_ref)

        for r, val in enumerate((dshift2, dscale2, dgate2, dpre2, dpost2)):
            stat_ref[r:r + 1, :] += val
        loss_ref[...] += part

    lat = lambda i: (i, 0)
    fixed = lambda i: (0, 0)
    bf = lambda w: jax.ShapeDtypeStruct((rows, w), BF16)
    return pl.pallas_call(
        body, name=name, grid=(rows // TM,),
        out_shape=[jax.ShapeDtypeStruct((8, 128), F32), jax.ShapeDtypeStruct((rows, d), F32), bf(d), bf(dff), bf(dff),
                   bf(dff), bf(d), jax.ShapeDtypeStruct((8, d), F32)],
        in_specs=[pl.BlockSpec((TM, d), lat), pl.BlockSpec((8, d), fixed), pl.BlockSpec((8, d), fixed)]
        + [VMEM_SPEC] * 3 + [pl.BlockSpec((TM, d), lat)],
        out_specs=[pl.BlockSpec((8, 128), fixed), pl.BlockSpec((TM, d), lat), pl.BlockSpec((TM, d), lat),
                   pl.BlockSpec((TM, dff), lat), pl.BlockSpec((TM, dff), lat), pl.BlockSpec((TM, dff), lat),
                   pl.BlockSpec((TM, d), lat), pl.BlockSpec((8, d), fixed)],
        compiler_params=_cparams(dimension_semantics=("arbitrary",)),
    )(z2, modx, norms, w_gate, w_up, w_down, target)


def _mixer_tail_bwd(x_lat, p, o_list, dz2, y1, modx, norms, onorms, w_br_hg, w_br_gla, w_out, n_ctx_tiles, n_tiles,
                    name):
    rows, d = x_lat.shape
    total = n_tiles * TM

    def body(x_ref, ofw_hg, obw_hg, ofw_gla, obw_gla, p_hgate, p_ggate, p_ghg_a, p_ghg_b, p_ggla_a, p_ggla_b,
             dz2_ref, y1_ref, modx_ref, norm_ref, on_ref, wbh_ref, wbg_ref, wout_ref,
             dohg_ref, dogla_ref, dhgate_ref, dggate_ref, dghg_ref, dggla_ref, dy1_ref, dbhg_ref, dbgla_ref,
             stat_ref):
        i = pl.program_id(0)

        @pl.when(i == 0)
        def _():
            stat_ref[...] = jnp.zeros_like(stat_ref)

        @pl.when(i < n_ctx_tiles)
        def _():
            for ref in (dohg_ref, dogla_ref, dhgate_ref, dggate_ref, dghg_ref, dggla_ref):
                ref[...] = jnp.zeros_like(ref)

        @pl.when(i >= n_ctx_tiles)
        def _():
            post1, gate1 = norm_ref[1:2, :], modx_ref[2:3, :]
            hg_on, gla_on = on_ref[0:1, 0:HD], on_ref[1:2, 0:HD]
            p_gate_hg = jnp.concatenate([p_ghg_a[...], p_ghg_b[...]], axis=1)
            p_gate_gla = jnp.concatenate([p_ggla_a[...], p_ggla_b[...]], axis=1)
            ph, pg = p_hgate[...], p_ggate[...]
            t = _mixer_tail(x_ref[...], ofw_hg[...] + obw_hg[...], ofw_gla[...] + obw_gla[...], ph, pg,
                            p_gate_hg, p_gate_gla, hg_on, gla_on, wbh_ref[...], wbg_ref[...], wout_ref[...])
            dz2 = dz2_ref[...]
            n1, r1 = _rms(y1_ref[...])
            dgate1 = _colsum(dz2 * n1 * post1)
            tt = dz2 * gate1
            dpost1 = _colsum(tt * n1)
            dy1 = _rms_bwd(tt * post1, n1, r1).astype(BF16)
            dy1_ref[...] = dy1
            dmerged = _dot_nt(dy1, wout_ref[...])
            dghg_ref[...] = dmerged * t["b_hg"] * t["s_hg"] * (1.0 - t["s_hg"])
            dggla_ref[...] = dmerged * t["b_gla"] * t["s_gla"] * (1.0 - t["s_gla"])
            db_hg = (dmerged * t["s_hg"]).astype(BF16)
            db_gla = (dmerged * t["s_gla"]).astype(BF16)
            dbhg_ref[...] = db_hg
            dbgla_ref[...] = db_gla
            don_acc = []
            for (db, wb, pgate, on, ns, rs, gain, gate_ref, do_ref) in (
                    (db_hg, wbh_ref, ph, t["on_hg"], t["n_hg"], t["r_hg"], hg_on, dhgate_ref, dohg_ref),
                    (db_gla, wbg_ref, pg, t["on_gla"], t["n_gla"], t["r_gla"], gla_on, dggate_ref, dogla_ref)):
                dog = _dot_nt(db, wb[...])
                gate_ref[...] = dog * on * _dsilu(pgate)
                don = dog * _silu(pgate)
                acc = jnp.zeros((1, HD), F32)
                for h in range(NH):
                    sl = slice(h * HD, (h + 1) * HD)
                    acc = acc + _colsum(don[:, sl] * ns[h])
                    do_ref[:, sl] = _rms_bwd(don[:, sl] * gain, ns[h], rs[h])
                don_acc.append(acc)
            stat_ref[0:1, :] += dgate1
            stat_ref[1:2, :] += dpost1
            stat_ref[2:3, 0:HD] += don_acc[0]
            stat_ref[2:3, HD:2 * HD] += don_acc[1]

    lat = lambda i: (jnp.maximum(i - n_ctx_tiles, 0), 0)
    full = lambda i: (i, 0)
    fixed = lambda i: (0, 0)

    def pcol(blk):
        return pl.BlockSpec((TM, HW), lambda i: (i, blk))

    in_specs = ([pl.BlockSpec((TM, d), lat)] + [pl.BlockSpec((TM, HW), full)] * 4
                + [pcol(C_HGATE), pcol(C_GGATE), pcol(9), pcol(10), pcol(11), pcol(12)]
                + [pl.BlockSpec((TM, d), lat), pl.BlockSpec((TM, d), lat)]
                + [pl.BlockSpec((8, d), fixed)] * 3 + [VMEM_SPEC] * 3)
    f = lambda w: jax.ShapeDtypeStruct((total, w), F32)
    out_shape = [f(HW), f(HW), f(HW), f(HW), f(d), f(d), jax.ShapeDtypeStruct((rows, d), BF16),
                 jax.ShapeDtypeStruct((rows, d), BF16), jax.ShapeDtypeStruct((rows, d), BF16),
                 jax.ShapeDtypeStruct((8, d), F32)]
    out_specs = ([pl.BlockSpec((TM, HW), full)] * 4 + [pl.BlockSpec((TM, d), full)] * 2
                 + [pl.BlockSpec((TM, d), lat)] * 3 + [pl.BlockSpec((8, d), fixed)])
    return pl.pallas_call(
        body, name=name, grid=(n_tiles,), out_shape=out_shape, in_specs=in_specs, out_specs=out_specs,
        compiler_params=_cparams(dimension_semantics=("arbitrary",)),
    )(x_lat, *o_list, p, p, p, p, p, p, dz2, y1, modx, norms, onorms, w_br_hg, w_br_gla, w_out)


def _in_projection_bwd(z, dz2, modc, modx, pre1, w_r, pieces, n_ctx_tiles, name):
    rows, d = z.shape
    lat_rows = dz2.shape[0]
    width = w_r.shape[0]
    n_pieces = len(pieces)

    def body(*refs):
        z_ref, dz2_ref, modc_ref, modx_ref, pre_ref, w_ref = refs[:6]
        (dhq_f, dhq_b, dhi_f, dhi_b, dhf_f, dhf_b, dhgate, dgq_f, dgq_b, dgk_f, dgk_b, dgv_f, dgv_b, dggate,
         dghg, dggla, dlr_f, dlr_b) = refs[6:6 + n_pieces]
        dp_ref, gx_ref, stat_ref = refs[6 + n_pieces:]
        i = pl.program_id(0)
        is_ctx = i < n_ctx_tiles
        sections = [
            (0, dhq_f[...] + dhq_b[...]), (HW, dhi_f[...] + dhi_b[...]), (2 * HW, dhf_f[...]), (3 * HW, dhf_b[...]),
            (4 * HW, dhgate[...]), (5 * HW, dgq_f[...] + dgq_b[...]), (6 * HW, dgk_f[...] + dgk_b[...]),
            (7 * HW, dgv_f[...] + dgv_b[...]), (8 * HW, dggate[...]),
            (9 * HW, dghg[:, 0:HW]), (10 * HW, dghg[:, HW:2 * HW]),
            (11 * HW, dggla[:, 0:HW]), (12 * HW, dggla[:, HW:2 * HW]), (OFF_LR, dlr_f[...] + dlr_b[...])]
        dh = jnp.zeros((TM, d), F32)
        for off, val in sections:
            w = val.shape[1]
            vb = val.astype(BF16)
            dp_ref[:, off:off + w] = vb
            dh = dh + jnp.dot(vb, w_ref[off:off + w, :], preferred_element_type=F32)
        n, r = _rms(z_ref[...])
        pre = pre_ref[...]
        scale = jnp.where(is_ctx, modc_ref[1:2, :], modx_ref[1:2, :])
        nw = n * pre
        dshift = _colsum(dh)
        dscale = _colsum(dh * nw)
        dnw = dh * (1.0 + scale)
        dpre = _colsum(dnw * n)
        gx_ref[...] = dz2_ref[...] + _rms_bwd(dnw * pre, n, r)
        zero = jnp.zeros((1, d), F32)

        @pl.when(i == 0)
        def _():
            stat_ref[...] = jnp.zeros_like(stat_ref)

        stat_ref[0:1, :] += jnp.where(is_ctx, zero, dshift)
        stat_ref[1:2, :] += jnp.where(is_ctx, zero, dscale)
        stat_ref[2:3, :] += jnp.where(is_ctx, dshift, zero)
        stat_ref[3:4, :] += jnp.where(is_ctx, dscale, zero)
        stat_ref[4:5, :] += dpre

    full = lambda i: (i, 0)
    lat = lambda i: (jnp.maximum(i - n_ctx_tiles, 0), 0)
    fixed = lambda i: (0, 0)
    piece_specs = [pl.BlockSpec((TM, a.shape[1]), full) for a in pieces]
    in_specs = [pl.BlockSpec((TM, d), full), pl.BlockSpec((TM, d), lat), pl.BlockSpec((8, d), fixed),
                pl.BlockSpec((8, d), fixed), pl.BlockSpec((1, d), fixed), VMEM_SPEC] + piece_specs
    return pl.pallas_call(
        body, name=name, grid=(rows // TM,),
        out_shape=[jax.ShapeDtypeStruct((rows, width), BF16), jax.ShapeDtypeStruct((lat_rows, d), F32),
                   jax.ShapeDtypeStruct((8, d), F32)],
        in_specs=in_specs,
        out_specs=[pl.BlockSpec((TM, width), full), pl.BlockSpec((TM, d), lat), pl.BlockSpec((8, d), fixed)],
        compiler_params=_cparams(dimension_semantics=("arbitrary",)),
    )(z, dz2, modc, modx, pre1, w_r, *pieces)


def _transposed_lhs_matmul(x_ref, dy_ref, o_ref, xt_ref):
    @pl.when(pl.program_id(1) == 0)
    def _():
        xt_ref[...] = x_ref[...].T

    o_ref[...] = jnp.dot(xt_ref[...], dy_ref[...], preferred_element_type=F32)


def _weight_grad(xs, dy, name, tk=None, tn=512, k_first=0, k_tiles=None):
    rows = dy.shape[0]
    n = dy.shape[1]
    tn_ = min(tn, n)
    tk_ = xs.shape[1] if tk is None else tk
    k_tiles = xs.shape[1] // tk_ if k_tiles is None else k_tiles
    k = k_tiles * tk_

    return pl.pallas_call(
        functools.partial(_transposed_lhs_matmul), name=name, grid=(k_tiles, n // tn_),
        out_shape=jax.ShapeDtypeStruct((k, n), F32),
        in_specs=[pl.BlockSpec((rows, tk_), lambda i, j: (0, i + k_first)),
                  pl.BlockSpec((rows, tn_), lambda i, j: (0, j))],
        out_specs=pl.BlockSpec((tk_, tn_), lambda i, j: (i, j)),
        scratch_shapes=[pltpu.VMEM((tk_, rows), BF16)],
        compiler_params=_cparams(dimension_semantics=("parallel", "arbitrary")),
    )(xs, dy)


def _running_sum(x, fw):
    c = x.shape[0]
    row = lax.broadcasted_iota(jnp.int32, (c, 1), 0)
    s = 1
    while s < c:
        if fw:
            x = x + jnp.where(row >= s, pltpu.roll(x, s, axis=0), 0.0)
        else:
            x = x + jnp.where(row < c - s, pltpu.roll(x, c - s, axis=0), 0.0)
        s *= 2
    return x


def _chunk_terms(q, k, g, fw):
    c = CHUNK
    r = lax.broadcasted_iota(jnp.int32, (c, c), 0)
    s = lax.broadcasted_iota(jnp.int32, (c, c), 1)
    causal = (s <= r) if fw else (s >= r)
    causal_t = (s >= r) if fw else (s <= r)
    cum = _running_sum(g, fw)
    row = lax.broadcasted_iota(jnp.int32, (c, 1), 0)
    pos = row if fw else (c - 1 - row)
    starts = [None]
    for j in range(1, NSUB):
        rj = SUB * j - 1 if fw else c - SUB * j
        starts.append(cum[rj:rj + 1, :])
    in_blk = [(pos >= SUB * j) & (pos < SUB * (j + 1)) for j in range(NSUB)]
    e = [jnp.exp(cum)]
    for j in range(1, NSUB):
        e.append(jnp.exp(jnp.where(pos >= SUB * j, cum - starts[j], -1e30)))
    own = jnp.zeros_like(cum)
    for j in range(1, NSUB):
        own = own + jnp.where(in_blk[j], starts[j], 0.0)
    kscale = jnp.exp(own - cum)
    rend = c - 1 if fw else 0
    cend = cum[rend:rend + 1, :]
    tail = jnp.exp(cend - cum)
    qcat = jnp.concatenate([q * e[j] for j in range(NSUB)], axis=1).astype(BF16)
    kt = k * kscale
    km = jnp.concatenate([jnp.where(in_blk[j], kt, 0.0) for j in range(NSUB)], axis=1).astype(BF16)
    return dict(causal=causal, causal_t=causal_t, e=e, in_blk=in_blk, kscale=kscale, cend=cend, tail=tail,
                qcat=qcat, km=km)


def _chunk_fwd(q, k, v, g, st0, fw):
    t = _chunk_terms(q, k, g, fw)
    a = jnp.where(t["causal"], _dot_nt(t["qcat"], t["km"]), 0.0)
    o = _dot(a, v) + _dot_nt(t["qcat"][:, 0:HD], st0)
    st1 = st0 * jnp.exp(t["cend"]) + _dot_tn(v, k * t["tail"])
    return o, st1


def _chunk_bwd(q, k, v, g, st0, do, dst1, fw):
    t = _chunk_terms(q, k, g, fw)
    qcat, km, e = t["qcat"], t["km"], t["e"]
    a_t = jnp.where(t["causal_t"], _dot_nt(km, qcat), 0.0)
    ktail = k * t["tail"]
    dv = _dot(a_t, do) + _dot_nt(ktail, dst1)
    da = jnp.where(t["causal"], _dot_nt(do, v), 0.0)
    da_t = jnp.where(t["causal_t"], _dot_nt(v, do), 0.0)
    dqcat = _dot(da, km)
    dq_inter = e[0] * _dot(do, st0)
    dq = dq_inter
    for j in range(NSUB):
        dq = dq + e[j] * dqcat[:, j * HD:(j + 1) * HD]
    dkm = _dot(da_t, qcat)
    dkt = jnp.zeros_like(k)
    for j in range(NSUB):
        dkt = dkt + jnp.where(t["in_blk"][j], dkm[:, j * HD:(j + 1) * HD], 0.0)
    dk_inter = _dot(v, dst1) * t["tail"]
    dk = dkt * t["kscale"] + dk_inter
    dcum = q * dq_inter - k * dk_inter
    for j in range(NSUB):
        sl = slice(j * HD, (j + 1) * HD)
        dcum = dcum + qcat[:, sl].astype(F32) * dqcat[:, sl] - km[:, sl].astype(F32) * dkm[:, sl]
    ecend = jnp.exp(t["cend"])
    end = ecend * _colsum(st0 * dst1) + _colsum(k * dk_inter)
    dg = _running_sum(dcum, not fw) + end
    dst0 = dst1 * ecend + _dot_tn(do, q * e[0])
    return dq, dk, dv, dg, dst0


def _chunk_index(step, n_ctx_chunks, n_chunks, fw):
    if fw:
        return step
    return jnp.where(step < n_ctx_chunks, n_ctx_chunks - 1 - step, n_chunks - 1 + n_ctx_chunks - step)


def _hg_inputs(hq, hf, lbv, d_idx, sl):
    lb = _sigmoid(lbv[d_idx:d_idx + 1, sl] - lbv[2 + d_idx:3 + d_idx, sl])
    sg = _sigmoid(hf)
    f = lb + (1.0 - lb) * sg
    return _silu(hq), 1.0 - f, jnp.log(f), f, sg, lb


def _scan_fwd(p, side, n_ctx_chunks, fw, branch, name):
    rows = p.shape[0]
    n_chunks = rows // CHUNK
    d_idx = 0 if fw else 1
    hg = branch == "hg"
    cols = (C_HQ, C_HI, C_HF_FW + d_idx) if hg else (C_GQ, C_GK, C_GV)

    def body(*refs):
        if hg:
            a_ref, b_ref, c_ref, lb_ref, o_ref, st_ref, state = refs
        else:
            a_ref, b_ref, c_ref, lr_ref, wgk_ref, bgk_ref, o_ref, st_ref, state = refs
            logits = _dot(lr_ref[...], wgk_ref[...]) + bgk_ref[...]
            g_all = _log_sigmoid(logits) * (1.0 / GATE_NORM)

        @pl.when(pl.program_id(0) == 0)
        def _():
            state[...] = jnp.zeros_like(state)

        for h in range(NH):
            sl = slice(h * HD, (h + 1) * HD)
            if hg:
                q, k, g, _, _, _ = _hg_inputs(a_ref[:, sl], c_ref[:, sl], lb_ref[...], d_idx, sl)
                v = b_ref[:, sl]
            else:
                q, k, v, g = a_ref[:, sl] * (HD ** -0.5), b_ref[:, sl], c_ref[:, sl], g_all[:, sl]
            st0 = state[h]
            st_ref[0, h] = st0
            o, st1 = _chunk_fwd(q, k, v, g, st0, fw)
            o_ref[:, sl] = o
            state[h] = st1

    def cmap(blk):
        return pl.BlockSpec((CHUNK, HW), lambda j: (_chunk_index(j, n_ctx_chunks, n_chunks, fw), blk))

    fixed = lambda j: (0, 0)
    in_specs = [cmap(cols[0]), cmap(cols[1]), cmap(cols[2])]
    if hg:
        in_specs += [pl.BlockSpec((4, HW), fixed)]
        args = (p, p, p, side)
    else:
        in_specs += [pl.BlockSpec((CHUNK, 128), lambda j: (_chunk_index(j, n_ctx_chunks, n_chunks, fw), OFF_LR // 128)),
                     pl.BlockSpec((128, HW), fixed), pl.BlockSpec((1, HW), fixed)]
        args = (p, p, p, p, side[0], side[1])
    return pl.pallas_call(
        body, name=name, grid=(n_chunks,),
        out_shape=[jax.ShapeDtypeStruct((rows, HW), F32), jax.ShapeDtypeStruct((n_chunks, NH, HD, HD), F32)],
        in_specs=in_specs,
        out_specs=[pl.BlockSpec((CHUNK, HW), lambda j: (_chunk_index(j, n_ctx_chunks, n_chunks, fw), 0)),
                   pl.BlockSpec((1, NH, HD, HD), lambda j: (_chunk_index(j, n_ctx_chunks, n_chunks, fw), 0, 0, 0))],
        scratch_shapes=[pltpu.VMEM((NH, HD, HD), F32)],
        compiler_params=_cparams(dimension_semantics=("arbitrary",)),
    )(*args)


def _scan_bwd(p, side, states, d_o, n_ctx_chunks, fw, branch, name):
    rows = p.shape[0]
    n_chunks = rows // CHUNK
    d_idx = 0 if fw else 1
    hg = branch == "hg"
    cols = (C_HQ, C_HI, C_HF_FW + d_idx) if hg else (C_GQ, C_GK, C_GV)

    def body(*refs):
        if hg:
            a_ref, b_ref, c_ref, lb_ref, st_ref, do_ref, da_ref, db_ref, dc_ref, dlb_ref, dstate = refs
        else:
            (a_ref, b_ref, c_ref, lr_ref, wgk_ref, bgk_ref, st_ref, do_ref, da_ref, db_ref, dc_ref, dlr_ref,
             dwgk_ref, dbias_ref, dstate) = refs
            lr = lr_ref[...]
            logits = _dot(lr, wgk_ref[...]) + bgk_ref[...]
            g_all = _log_sigmoid(logits) * (1.0 / GATE_NORM)

        @pl.when(pl.program_id(0) == 0)
        def _():
            dstate[...] = jnp.zeros_like(dstate)
            if hg:
                dlb_ref[...] = jnp.zeros_like(dlb_ref)
            else:
                dwgk_ref[...] = jnp.zeros_like(dwgk_ref)
                dbias_ref[...] = jnp.zeros_like(dbias_ref)

        dg_parts = []
        for h in range(NH):
            sl = slice(h * HD, (h + 1) * HD)
            if hg:
                hq, hf = a_ref[:, sl], c_ref[:, sl]
                q, k, g, f, sg, lb = _hg_inputs(hq, hf, lb_ref[...], d_idx, sl)
                v = b_ref[:, sl]
            else:
                q, k, v, g = a_ref[:, sl] * (HD ** -0.5), b_ref[:, sl], c_ref[:, sl], g_all[:, sl]
            dq, dk, dv, dg, dst0 = _chunk_bwd(q, k, v, g, st_ref[0, h], do_ref[:, sl], dstate[h], fw)
            dstate[h] = dst0
            if hg:
                da_ref[:, sl] = dq * _dsilu(hq)
                db_ref[:, sl] = dv
                df = dg / f - dk
                dc_ref[:, sl] = df * (1.0 - lb) * sg * (1.0 - sg)
                dlb_ref[0:1, sl] += _colsum(df * (1.0 - sg))
            else:
                da_ref[:, sl] = dq * (HD ** -0.5)
                db_ref[:, sl] = dk
                dc_ref[:, sl] = dv
                dg_parts.append(dg)
        if not hg:
            dlogits = jnp.concatenate(dg_parts, axis=1) * (1.0 / GATE_NORM) * (1.0 - _sigmoid(logits))
            dlr_ref[...] = _dot_nt(dlogits, wgk_ref[...])
            dwgk_ref[...] += _dot_tn(lr, dlogits)
            dbias_ref[0:1, :] += _colsum(dlogits)

    def chunk_of(j):
        return _chunk_index(n_chunks - 1 - j, n_ctx_chunks, n_chunks, fw)

    def cmap(blk, width=HW):
        return pl.BlockSpec((CHUNK, width), lambda j: (chunk_of(j), blk))

    fixed = lambda j: (0, 0)
    st_spec = pl.BlockSpec((1, NH, HD, HD), lambda j: (chunk_of(j), 0, 0, 0))
    big = jax.ShapeDtypeStruct((rows, HW), F32)
    if hg:
        in_specs = [cmap(cols[0]), cmap(cols[1]), cmap(cols[2]), pl.BlockSpec((4, HW), fixed), st_spec, cmap(0)]
        args = (p, p, p, side, states, d_o)
        out_shape = [big, big, big, jax.ShapeDtypeStruct((8, HW), F32)]
        out_specs = [cmap(0), cmap(0), cmap(0), pl.BlockSpec((8, HW), fixed)]
    else:
        in_specs = [cmap(cols[0]), cmap(cols[1]), cmap(cols[2]), cmap(OFF_LR // 128, 128),
                    pl.BlockSpec((128, HW), fixed), pl.BlockSpec((1, HW), fixed), st_spec, cmap(0)]
        args = (p, p, p, p, side[0], side[1], states, d_o)
        out_shape = [big, big, big, jax.ShapeDtypeStruct((rows, 128), F32), jax.ShapeDtypeStruct((128, HW), F32),
                     jax.ShapeDtypeStruct((8, HW), F32)]
        out_specs = [cmap(0), cmap(0), cmap(0), cmap(0, 128), pl.BlockSpec((128, HW), fixed),
                     pl.BlockSpec((8, HW), fixed)]
    return pl.pallas_call(
        body, name=name, grid=(n_chunks,), out_shape=out_shape, in_specs=in_specs, out_specs=out_specs,
        scratch_shapes=[pltpu.VMEM((NH, HD, HD), F32)],
        compiler_params=_cparams(dimension_semantics=("arbitrary",)),
    )(*args)


SMALL_ROWS = 56
ROWS_MOD_X = (0, 1, 8, 16, 17, 18)
ROWS_MOD_C = (2, 3)
ROW_PRE1, ROW_POST1, ROW_ONORM, ROW_PRE2, ROW_POST2, ROW_LB, ROW_BGK, ROW_WGK = 4, 9, 10, 19, 20, 24, 32, 40


def _reduce_small(gathered, lb_full, name):
    _, _, d = gathered.shape

    def body(g_ref, lb_ref, sum_ref, dmod_ref, dbmod_ref, dlb_ref):
        total = g_ref[0]
        for b in range(1, N_DEV):
            total = total + g_ref[b]
        sum_ref[...] = total
        dmod_ref[...] = jnp.zeros_like(dmod_ref)
        for m in range(N_MOD):
            col = slice(m * d, (m + 1) * d)
            acc = jnp.zeros((1, d), F32)
            for b in range(N_DEV):
                row = g_ref[b, ROWS_MOD_X[m]:ROWS_MOD_X[m] + 1, :]
                dmod_ref[b:b + 1, col] = row
                acc = acc + row
            if m < 2:
                ctx_row = total[ROWS_MOD_C[m]:ROWS_MOD_C[m] + 1, :]
                dmod_ref[8:9, col] = ctx_row
                acc = acc + ctx_row
            dbmod_ref[:, col] = acc
        lbv = lb_ref[...]
        for dd in range(2):
            lb = _sigmoid(lbv[dd:dd + 1, :] - lbv[2 + dd:3 + dd, :])
            gl = total[ROW_LB:ROW_LB + 1, dd * HW:(dd + 1) * HW] * lb * (1.0 - lb)
            dlb_ref[dd:dd + 1, :] = gl
            dlb_ref[2 + dd:3 + dd, :] = -gl

    return pl.pallas_call(
        body, name=name,
        out_shape=[jax.ShapeDtypeStruct((SMALL_ROWS, d), F32), jax.ShapeDtypeStruct((16, N_MOD * d), F32),
                   jax.ShapeDtypeStruct((1, N_MOD * d), F32), jax.ShapeDtypeStruct((4, HW), F32)],
        in_specs=[VMEM_SPEC] * 2, out_specs=[VMEM_SPEC] * 4, compiler_params=_cparams(),
    )(gathered, lb_full)


def _c_ctx_grad(gathered, c_ctx_row, name):
    def body(g_ref, c_ref, o_ref):
        acc = g_ref[0, 0:1, :]
        for chip in range(1, N_CHIP):
            acc = acc + g_ref[2 * chip, 0:1, :]
        o_ref[...] = acc * _dsilu(c_ref[...])

    return pl.pallas_call(
        body, name=name, out_shape=jax.ShapeDtypeStruct(c_ctx_row.shape, F32),
        in_specs=[VMEM_SPEC] * 2, out_specs=VMEM_SPEC, compiler_params=_cparams(),
    )(gathered, c_ctx_row)


def _relayout_w_in(w):
    pad = jnp.zeros((w.shape[0], 128 - 2 * RANK), w.dtype)
    return jnp.concatenate([w[:, :9 * HW], w[:, 9 * HW + 2 * RANK:], w[:, 9 * HW:9 * HW + 2 * RANK], pad], axis=1)


def _relayout_w_in_rows(wt):
    pad = jnp.zeros((128 - 2 * RANK, wt.shape[1]), wt.dtype)
    return jnp.concatenate([wt[:9 * HW], wt[9 * HW + 2 * RANK:], wt[9 * HW:9 * HW + 2 * RANK], pad], axis=0)


def _w_in_grad_rows(g_main, g_lr):
    return jnp.concatenate([g_main[:9 * HW], g_lr[:2 * RANK], g_main[9 * HW:]], axis=0)


def _w_in_grad_blocks(g_main, g_lr, n_blocks):
    lr0 = 9 * HW
    n = (g_main.shape[1] + 2 * RANK) // n_blocks

    def cols(lo, hi):
        out = []
        if lo < lr0:
            out.append(g_main[:, lo:min(hi, lr0)])
        if hi > lr0 and lo < lr0 + 2 * RANK:
            out.append(g_lr[:, max(lo, lr0) - lr0:min(hi, lr0 + 2 * RANK) - lr0])
        if hi > lr0 + 2 * RANK:
            out.append(g_main[:, max(lo, lr0 + 2 * RANK) - 2 * RANK:hi - 2 * RANK])
        return out

    return jnp.stack([jnp.concatenate(cols(j * n, (j + 1) * n), axis=1) for j in range(n_blocks)])


def _blocked(full, n_blocks):
    k, n = full.shape
    return full.reshape(k, n_blocks, n // n_blocks).transpose(1, 0, 2)


def _unblocked(blocks):
    nb, k, n = blocks.shape
    return blocks.transpose(1, 0, 2).reshape(k, nb * n)


def _sample_front(x0, ctx0, modc, modx, norm_pre1, lb_full, gla_side, w_in_r):
    ctx_len = ctx0.shape[0]
    n_ctx_tiles = ctx_len // TM
    n_ctx_chunks = ctx_len // CHUNK
    z = jnp.concatenate([ctx0, x0], axis=0)
    h1, p = _in_projection(z, modc, modx, norm_pre1, w_in_r, n_ctx_tiles, "in_projection")
    o_hg_fw, st_hg_fw = _scan_fwd(p, lb_full, n_ctx_chunks, True, "hg", "scan_hg_fw")
    o_hg_bw, st_hg_bw = _scan_fwd(p, lb_full, n_ctx_chunks, False, "hg", "scan_hg_bw")
    o_gla_fw, st_gla_fw = _scan_fwd(p, gla_side[0], n_ctx_chunks, True, "gla", "scan_gla_fw")
    o_gla_bw, st_gla_bw = _scan_fwd(p, gla_side[1], n_ctx_chunks, False, "gla", "scan_gla_bw")
    return dict(z=z, h1=h1, p=p, o_list=[o_hg_fw, o_hg_bw, o_gla_fw, o_gla_bw],
                states=[st_hg_fw, st_hg_bw, st_gla_fw, st_gla_bw])


def _sample_back(reduce, front, x0, ctx0, target0, modc, modx, norm_pre1, norms, onorms, lb_full, gla_side, w_in_r,
                 wbh, wbg, wout, wg, wu, wd):
    seq, d = x0.shape
    ctx_len = ctx0.shape[0]
    n_ctx_tiles = ctx_len // TM
    n_tiles = (ctx_len + seq) // TM
    n_ctx_chunks = ctx_len // CHUNK
    z, h1, p, o_list = front["z"], front["h1"], front["p"], front["o_list"]
    st_hg_fw, st_hg_bw, st_gla_fw, st_gla_bw = front["states"]
    z2, y1, merged, og_hg, og_gla = _mixer_tail_fwd(x0, p, o_list, modx, norms, onorms, wbh, wbg, wout, n_ctx_tiles,
                                                    "mixer_tail")
    loss_part, dz2, h2, a_act, du, dv, dy2, stat_ffn = _ffn_fwd_bwd(z2, modx, norms, wg, wu, wd, target0, "ffn")
    dff = wg.shape[0]
    tok = reduce("ffn", [_weight_grad(du, h2, "grad_w_ff_gate", tk=dff // 2, tn=d),
                         _weight_grad(dv, h2, "grad_w_ff_up", tk=dff // 2, tn=d),
                         _weight_grad(a_act, dy2, "grad_w_ff_down", tk=dff // 2)])

    (d_ohg, d_ogla, d_hgate, d_ggate, d_ghg, d_ggla, dy1, db_hg, db_gla, stat_mix) = _mixer_tail_bwd(
        x0, p, o_list, dz2, y1, modx + tok, norms, onorms, wbh, wbg, wout, n_ctx_tiles, n_tiles, "mixer_tail_bwd")
    tok = reduce("mix", [_weight_grad(og_hg, db_hg, "grad_w_br_hg"), _weight_grad(og_gla, db_gla, "grad_w_br_gla"),
                         _weight_grad(merged, dy1, "grad_w_out")])
    lb_b = lb_full + tok
    gla_b = [(wgk, bias + tok) for wgk, bias in gla_side]
    dhq_f, dhi_f, dhf_f, dlb_f = _scan_bwd(p, lb_b, st_hg_fw, d_ohg, n_ctx_chunks, True, "hg", "scan_hg_fw_bwd")
    dhq_b, dhi_b, dhf_b, dlb_b = _scan_bwd(p, lb_b, st_hg_bw, d_ohg, n_ctx_chunks, False, "hg", "scan_hg_bw_bwd")
    dgq_f, dgk_f, dgv_f, dlr_f, dwgk_f, dbgk_f = _scan_bwd(p, gla_b[0], st_gla_fw, d_ogla, n_ctx_chunks, True, "gla",
                                                           "scan_gla_fw_bwd")
    dgq_b, dgk_b, dgv_b, dlr_b, dwgk_b, dbgk_b = _scan_bwd(p, gla_b[1], st_gla_bw, d_ogla, n_ctx_chunks, False, "gla",
                                                           "scan_gla_bw_bwd")
    pieces = [dhq_f, dhq_b, dhi_f, dhi_b, dhf_f, dhf_b, d_hgate, dgq_f, dgq_b, dgk_f, dgk_b, dgv_f, dgv_b, d_ggate,
              d_ghg, d_ggla, dlr_f, dlr_b]
    dp, grad_x, stat_in = _in_projection_bwd(z, dz2, modc, modx, norm_pre1, w_in_r, pieces, n_ctx_tiles,
                                             "in_projection_bwd")

    reduce("small", dict(stat_in=stat_in, stat_mix=stat_mix, stat_ffn=stat_ffn, dlb=(dlb_f, dlb_b),
                         dwgk=(dwgk_f, dwgk_b), dbgk=(dbgk_f, dbgk_b)))
    g_in_main = _weight_grad(dp, h1, "grad_w_in_main", tk=HW, tn=d, k_tiles=OFF_LR // HW)
    g_in_lr = _weight_grad(dp, h1, "grad_w_in_lr", tk=128, tn=d, k_first=OFF_LR // 128, k_tiles=1)
    reduce("in", [_w_in_grad_rows(g_in_main, g_in_lr)])
    return dict(
        loss_part=loss_part, grad_x=grad_x, stat_in=stat_in, stat_mix=stat_mix, stat_ffn=stat_ffn,
        dlb=(dlb_f, dlb_b), dwgk=(dwgk_f, dwgk_b), dbgk=(dbgk_f, dbgk_b))


def kernel(x, c, ctx, c_ctx, w_mod, b_mod, norm_pre1, norm_post1, norm_pre2, norm_post2, w_in, hg_lb, hg_onorm, gla_w_gk, gla_b_gk, gla_onorm, w_br_hg, w_br_gla, w_out, w_ff_gate, w_ff_up, w_ff_down, loss_target, m_c_ctx, m_w_mod, m_b_mod, m_norm_pre1, m_norm_post1, m_norm_pre2, m_norm_post2, m_w_in, m_hg_lb, m_hg_onorm, m_gla_w_gk, m_gla_b_gk, m_gla_onorm, m_w_br_hg, m_w_br_gla, m_w_out, m_w_ff_gate, m_w_ff_up, m_w_ff_down, v_c_ctx, v_w_mod, v_b_mod, v_norm_pre1, v_norm_post1, v_norm_pre2, v_norm_post2, v_w_in, v_hg_lb, v_hg_onorm, v_gla_w_gk, v_gla_b_gk, v_gla_onorm, v_w_br_hg, v_w_br_gla, v_w_out, v_w_ff_gate, v_w_ff_up, v_w_ff_down):
    seq, d = x.shape[1], x.shape[2]
    ctx_len = ctx.shape[1]
    assert seq % TM == 0 and ctx_len % TM == 0 and d == 2 * HW
    ax, ay, ac = lax.axis_index("x"), lax.axis_index("y"), lax.axis_index("c")
    chip = 2 * ax + ay
    dev = 2 * chip + ac
    c_arr = jnp.reshape(ac, (1,)).astype(jnp.int32)

    nc = d // 128
    pad8 = lambda a: jnp.pad(a, ((0, -a.shape[0] % 8), (0, 0)))
    small1 = jnp.concatenate([c.reshape(nc, 128), pad8(hg_lb.reshape(4, 128)), gla_w_gk.reshape(2 * RANK, 128),
                              pad8(gla_b_gk.reshape(2, 128))], axis=0)
    got1 = _allgather8(small1, "gather_small_params")
    c_all = got1[:, :nc, :].reshape(N_DEV, d)
    per_chip = got1[0::2]
    lb_full = per_chip[:, nc:nc + 4, :].transpose(1, 0, 2).reshape(4, HW)
    wgk_full = per_chip[:, nc + 8:nc + 8 + 2 * RANK, :].transpose(1, 0, 2).reshape(2, RANK, HW)
    bgk_full = per_chip[:, nc + 8 + 2 * RANK:nc + 10 + 2 * RANK, :].transpose(1, 0, 2).reshape(2, HW)
    wgk_pad = [jnp.zeros((128, HW), F32).at[dd * RANK:(dd + 1) * RANK].set(wgk_full[dd]) for dd in range(2)]
    bgk = [bgk_full[dd:dd + 1] for dd in range(2)]

    n_mod_cols = w_mod.shape[2]
    cond = jnp.concatenate([c_all, pad8(c_ctx.reshape(1, d))], axis=0)
    b_cols = lax.dynamic_slice(b_mod, (0, chip * n_mod_cols), (1, n_mod_cols))
    mod_part = _mod_forward(cond, w_mod[0], b_cols, "mod_forward")
    mod_got = _allgather8(mod_part, "gather_mod")
    mod_all = mod_got[0::2].transpose(1, 0, 2).reshape(16, N_CHIP * n_mod_cols)
    modx = pad8(lax.dynamic_slice(mod_all, (dev, 0), (1, N_MOD * d)).reshape(N_MOD, d))
    modc = pad8(mod_all[8].reshape(N_MOD, d))

    chip_arr = jnp.reshape(chip, (1,)).astype(jnp.int32)
    transposed = ("w_in", "w_ff_gate", "w_ff_up")
    view = lambda a, nm: a[0].T if nm in transposed else a[0]
    blocks = [_cast_into_blocks(chip_arr, view(w_, nm), "cast_" + nm) for w_, nm in (
        (w_in, "w_in"), (w_br_hg, "w_br_hg"), (w_br_gla, "w_br_gla"), (w_out, "w_out"), (w_ff_gate, "w_ff_gate"),
        (w_ff_up, "w_ff_up"), (w_ff_down, "w_ff_down"))]
    gathered_in = _gather_blocks(blocks[:1], "gather_w_in")
    sems, lands, token = _blocks_start(blocks[1:], "gather_rest_start")
    w_in_r = _relayout_w_in_rows(gathered_in[0].reshape(-1, d))

    norms = jnp.concatenate([norm_pre1, norm_post1, norm_pre2, norm_post2, jnp.zeros((4, d), F32)], axis=0)
    onorms = jnp.zeros((8, d), F32).at[0, :HD].set(hg_onorm[0]).at[1, :HD].set(gla_onorm[0])
    gla_side = [(wgk_pad[dd], bgk[dd]) for dd in range(2)]
    modx = modx + token[0, 0]
    front = _sample_front(x[0], ctx[0], modc, modx, norm_pre1, lb_full, gla_side, w_in_r)
    lands = _blocks_wait(sems, lands, front["o_list"][3], "gather_rest_wait")
    gathered = _blocks_finish(lands, "gather_rest_finish")
    wbh, wbg = _unblocked(gathered[0]), _unblocked(gathered[1])
    wout = gathered[2].reshape(d, d)
    wg, wu, wd = (gathered[i].reshape(-1, d) for i in (3, 4, 5))
    dff = wg.shape[0]
    groups = {"ffn": ["w_ff_gate", "w_ff_up", "w_ff_down"], "mix": ["w_br_hg", "w_br_gla", "w_out"], "in": ["w_in"]}
    row_sharded = {"w_out": d // N_CHIP, "w_ff_down": dff // N_CHIP, "w_ff_gate": dff // N_CHIP,
                   "w_ff_up": dff // N_CHIP, "w_in": w_in.shape[2]}
    in_flight = {}

    small = {}

    def reduce_small(stats):
        small2 = jnp.concatenate([
            stats["stat_in"], stats["stat_mix"], stats["stat_ffn"],
            jnp.concatenate(stats["dlb"], axis=1), jnp.concatenate(stats["dbgk"], axis=1),
            jnp.concatenate([stats["dwgk"][0][0:RANK], stats["dwgk"][1][RANK:2 * RANK]], axis=1)], axis=0)
        assert small2.shape[0] == SMALL_ROWS
        got2 = _allgather8(small2, "gather_small_grads")
        total, dmod_all, g_b_mod, g_lb_full = _reduce_small(got2, lb_full, "reduce_small")
        dmod_cols = lax.dynamic_slice(dmod_all, (0, chip * n_mod_cols), (16, n_mod_cols))
        g_w_mod, cctx_part = _mod_backward(cond, w_mod[0], dmod_cols, "mod_backward")
        got3 = _allgather8(cctx_part, "gather_c_ctx_grad")
        g_c_ctx = _c_ctx_grad(got3, c_ctx.reshape(1, d), "c_ctx_grad")
        small.update(total=total, g_b_mod=g_b_mod, g_lb_full=g_lb_full, g_w_mod=g_w_mod, g_c_ctx=g_c_ctx)

    def reduce(group, grads):
        if group == "small":
            return reduce_small(grads)
        nms = groups[group]
        full = [g.reshape(N_CHIP, row_sharded[nm], d) if nm in row_sharded else _blocked(g, N_CHIP)
                for g, nm in zip(grads, nms)]
        from_sibling = _send_other_half(full, "grads_to_sibling_" + group)
        pairs = [_pair_sum(c_arr, f, r_, "pair_sum_" + nm) for f, r_, nm in zip(full, from_sibling, nms)]
        after = [small["g_c_ctx"], small["total"]] if group == "in" else []
        sems_, pairs, lands_, token_ = _scatter_start(pairs, "grads_to_owner_start_" + group, after)
        in_flight[group] = (sems_, pairs, lands_, token_)
        return token_[0, 0]

    r = _sample_back(reduce, front, x[0], ctx[0], loss_target[0], modc, modx, norm_pre1, norms, onorms, lb_full,
                     gla_side, w_in_r, wbh, wbg, wout, wg, wu, wd)
    loss_part, grad_x, stat_in, stat_mix, stat_ffn = (r[k] for k in ("loss_part", "grad_x", "stat_in", "stat_mix",
                                                                     "stat_ffn"))
    (dlb_f, dlb_b), (dwgk_f, dwgk_b), (dbgk_f, dbgk_b) = r["dlb"], r["dwgk"], r["dbgk"]

    weights = dict(w_in=(w_in, m_w_in, v_w_in), w_br_hg=(w_br_hg, m_w_br_hg, v_w_br_hg),
                   w_br_gla=(w_br_gla, m_w_br_gla, v_w_br_gla), w_out=(w_out, m_w_out, v_w_out),
                   w_ff_gate=(w_ff_gate, m_w_ff_gate, v_w_ff_gate), w_ff_up=(w_ff_up, m_w_ff_up, v_w_ff_up),
                   w_ff_down=(w_ff_down, m_w_ff_down, v_w_ff_down))
    names = ["w_in", "w_br_hg", "w_br_gla", "w_out", "w_ff_gate", "w_ff_up", "w_ff_down"]
    big = {}

    def finish(group, after):
        sems_, pairs, lands_, _ = in_flight[group]
        pairs, lands_ = _scatter_wait(sems_, pairs, lands_, after, "grads_to_owner_wait_" + group)
        own_half = [_sum_owner(chip_arr, pr, g, "chip_sum_" + nm) for pr, g, nm in zip(pairs, lands_, groups[group])]
        other_half = _swap_with_sibling(own_half, "halves_to_sibling_" + group)
        for nm, own, oth in zip(groups[group], own_half, other_half):
            w_, m_, v_ = (view(a, nm) for a in weights[nm])
            res = _adamw_halves(c_arr, own, oth, w_, m_, v_, "adamw_" + nm)
            big[nm] = [r_.T[None] if nm in transposed else r_[None] for r_ in res]
        return big[groups[group][-1]][1]

    token_in = in_flight["in"][3]
    done_ffn = finish("ffn", [token_in])
    done_mix = finish("mix", [done_ffn])

    total, g_b_mod, g_lb_full, g_w_mod, g_c_ctx = (small[k] for k in ("total", "g_b_mod", "g_lb_full", "g_w_mod",
                                                                      "g_c_ctx"))
    g_pre1, g_post1, g_pre2, g_post2 = (total[r_:r_ + 1] for r_ in (ROW_PRE1, ROW_POST1, ROW_PRE2, ROW_POST2))
    g_hg_on, g_gla_on = total[ROW_ONORM:ROW_ONORM + 1, 0:HD], total[ROW_ONORM:ROW_ONORM + 1, HD:2 * HD]
    n_lb = hg_lb.shape[2]
    g_hg_lb = lax.dynamic_slice(g_lb_full, (0, chip * n_lb), (4, n_lb))
    g_bgk = lax.dynamic_slice(total[ROW_BGK:ROW_BGK + 1].reshape(2, HW), (0, chip * n_lb), (2, n_lb))
    g_wgk_full = total[ROW_WGK:ROW_WGK + RANK].reshape(RANK, 2, HW).transpose(1, 0, 2).reshape(2 * RANK, HW)
    g_wgk = lax.dynamic_slice(g_wgk_full, (0, chip * n_lb), (2 * RANK, n_lb))

    small_items = [
        (g_c_ctx, c_ctx.reshape(1, d), m_c_ctx.reshape(1, d), v_c_ctx.reshape(1, d)),
        (g_b_mod, b_mod, m_b_mod, v_b_mod),
        (g_pre1, norm_pre1, m_norm_pre1, v_norm_pre1),
        (g_post1, norm_post1, m_norm_post1, v_norm_post1),
        (g_pre2, norm_pre2, m_norm_pre2, v_norm_pre2),
        (g_post2, norm_post2, m_norm_post2, v_norm_post2),
        (g_hg_lb, hg_lb.reshape(4, n_lb), m_hg_lb.reshape(4, n_lb), v_hg_lb.reshape(4, n_lb)),
        (g_hg_on, hg_onorm, m_hg_onorm, v_hg_onorm),
        (g_wgk, gla_w_gk.reshape(2 * RANK, n_lb), m_gla_w_gk.reshape(2 * RANK, n_lb), v_gla_w_gk.reshape(2 * RANK, n_lb)),
        (g_bgk, gla_b_gk.reshape(2, n_lb), m_gla_b_gk.reshape(2, n_lb), v_gla_b_gk.reshape(2, n_lb)),
        (g_gla_on, gla_onorm, m_gla_onorm, v_gla_onorm),
    ]
    small_res = _adamw_whole(small_items, "adamw_small")
    mod_res = _adamw_tiled(g_w_mod, w_mod[0], m_w_mod[0], v_w_mod[0], "adamw_w_mod")
    finish("in", [done_mix, mod_res[0], small_res[0][0]])

    loss = lax.psum(loss_part[0, 0], ("x", "y", "c"))

    shapes = dict(c_ctx=c_ctx.shape, b_mod=b_mod.shape, norm_pre1=norm_pre1.shape, norm_post1=norm_post1.shape,
                  norm_pre2=norm_pre2.shape, norm_post2=norm_post2.shape, hg_lb=hg_lb.shape, hg_onorm=hg_onorm.shape,
                  gla_w_gk=gla_w_gk.shape, gla_b_gk=gla_b_gk.shape, gla_onorm=gla_onorm.shape)
    small_names = ["c_ctx", "b_mod", "norm_pre1", "norm_post1", "norm_pre2", "norm_post2", "hg_lb", "hg_onorm",
                   "gla_w_gk", "gla_b_gk", "gla_onorm"]
    grads, deltas, new_m, new_v = {}, {}, {}, {}
    for nm, item, res in zip(small_names, small_items, small_res):
        grads[nm] = item[0].reshape(shapes[nm])
        deltas[nm], new_m[nm], new_v[nm] = (r.reshape(shapes[nm]) for r in res)
    grads["w_mod"] = g_w_mod[None]
    deltas["w_mod"], new_m["w_mod"], new_v["w_mod"] = (r[None] for r in mod_res)
    for nm in names:
        grads[nm], deltas[nm], new_m[nm], new_v[nm] = big[nm]
    order = ["c_ctx", "w_mod", "b_mod", "norm_pre1", "norm_post1", "norm_pre2", "norm_post2", "w_in", "hg_lb",
             "hg_onorm", "gla_w_gk", "gla_b_gk", "gla_onorm", "w_br_hg", "w_br_gla", "w_out", "w_ff_gate", "w_ff_up",
             "w_ff_down"]
    return (loss, grad_x[None], *[grads[n] for n in order], *[deltas[n] for n in order],
            *[new_m[n] for n in order], *[new_v[n] for n in order])


def _weight_grad_cols(xs, dy, n_cols, name, tn=512):
    rows = dy.shape[0]
    k = tk = xs.shape[1]

    return pl.pallas_call(
        functools.partial(_transposed_lhs_matmul), name=name, grid=(k // tk, n_cols // tn),
        out_shape=jax.ShapeDtypeStruct((k, n_cols), F32),
        in_specs=[pl.BlockSpec((rows, tk), lambda i, j: (0, i)), pl.BlockSpec((rows, tn), lambda i, j: (0, j))],
        out_specs=pl.BlockSpec((tk, tn), lambda i, j: (i, j)),
        scratch_shapes=[pltpu.VMEM((tk, rows), BF16)],
        compiler_params=_cparams(dimension_semantics=("parallel", "arbitrary")),
    )(xs, dy)
```

```python
import functools

import jax
import jax.numpy as jnp
from jax import lax
from jax.experimental import pallas as pl
from jax.experimental.pallas import tpu as pltpu

F32 = jnp.float32
BF16 = jnp.bfloat16
HIGHEST = lax.Precision.HIGHEST
MESH = pl.DeviceIdType.MESH

EPS = 1e-6
CHUNK = 64
SUB = 16
NSUB = CHUNK // SUB
NH = 4
HD = 128
HW = NH * HD
RANK = 16
GATE_NORM = 16.0
N_MOD = 6
TM = 256
TM_FFN = 128
N_DEV = 8
N_CHIP = 4
VMEM_LIMIT = 56 * 1024 * 1024

ADAM_LR = 0.001
ADAM_B1 = 0.9
ADAM_B2 = 0.999
ADAM_EPS = 1e-08
ADAM_WD = 0.01
ADAM_STEP = 10

VMEM_SPEC = pl.BlockSpec(memory_space=pltpu.VMEM)
ANY_SPEC = pl.BlockSpec(memory_space=pl.ANY)
HBM_SPEC = pl.BlockSpec(memory_space=pltpu.HBM)
SEM_SPEC = pl.BlockSpec(memory_space=pltpu.SEMAPHORE)
EFFECT = pltpu.SideEffectType.DATAFLOW_SIDE_EFFECTING


def _cparams(**kw):
    return pltpu.CompilerParams(vmem_limit_bytes=VMEM_LIMIT, **kw)


def _dot(a, b):
    return jnp.dot(a.astype(BF16), b.astype(BF16), preferred_element_type=F32)


def _dot_nt(a, b):
    return lax.dot_general(a.astype(BF16), b.astype(BF16), (((1,), (1,)), ((), ())), preferred_element_type=F32)


def _dot_tn(a, b):
    return lax.dot_general(a.astype(BF16), b.astype(BF16), (((0,), (0,)), ((), ())), preferred_element_type=F32)


def _sigmoid(x):
    return 1.0 / (1.0 + jnp.exp(-x))


def _silu(x):
    return x * _sigmoid(x)


def _dsilu(x):
    s = _sigmoid(x)
    return s * (1.0 + x * (1.0 - s))


def _log_sigmoid(x):
    return jnp.minimum(x, 0.0) - jnp.log(1.0 + jnp.exp(-jnp.abs(x)))


def _colsum(a):
    return jnp.sum(a, axis=0, keepdims=True)


def _rms(a):
    r = lax.rsqrt(jnp.mean(a * a, axis=-1, keepdims=True) + EPS)
    return a * r, r


def _rms_bwd(dn, n, r):
    return r * (dn - n * jnp.mean(dn * n, axis=-1, keepdims=True))


def _place():
    x, y, c = lax.axis_index("x"), lax.axis_index("y"), lax.axis_index("c")
    chips = [(1 - x, y), (x, 1 - y), (1 - x, 1 - y)]
    return x, y, c, chips


def _allgather8(v, name):
    rows, cols = v.shape

    def body(x_ref, out_ref, send_sems, recv_sems, local_sem):
        x, y, c, chips = _place()
        me, sibling = (x, y, c), (x, y, 1 - c)

        def blk(px, py, pc):
            return out_ref.at[4 * px + 2 * py + pc]

        def copy(k, block, to, src=None):
            return pltpu.make_async_remote_copy(
                src_ref=blk(*block) if src is None else src, dst_ref=blk(*block),
                send_sem=send_sems.at[k], recv_sem=recv_sems.at[k], device_id=to, device_id_type=MESH)

        mine = pltpu.make_async_copy(x_ref, blk(*me), local_sem)
        mine.start()
        first = [copy(0, me, sibling, src=x_ref)]
        first += [copy(1 + j, me, (*chip, c), src=x_ref) for j, chip in enumerate(chips)]
        for cp in first:
            cp.start()
        passed = [copy(4 + j, (*chip, c), sibling) for j, chip in enumerate(chips)]
        for j, chip in enumerate(chips):
            copy(1 + j, (*chip, c), me).wait_recv()
            passed[j].start()
        copy(0, sibling, me).wait_recv()
        for j, chip in enumerate(chips):
            copy(4 + j, (*chip, 1 - c), me).wait_recv()
        for cp in first + passed:
            cp.wait_send()
        mine.wait()

    return pl.pallas_call(
        body, name=name,
        out_shape=jax.ShapeDtypeStruct((N_DEV, rows, cols), v.dtype),
        in_specs=[VMEM_SPEC], out_specs=VMEM_SPEC,
        scratch_shapes=[pltpu.SemaphoreType.DMA((7,)), pltpu.SemaphoreType.DMA((7,)), pltpu.SemaphoreType.DMA],
    )(v)


def _cast_into_blocks(chip_arr, w, name):
    rows, cols = w.shape
    tr = _row_tile(rows, 16, 256)

    def body(chip_ref, w_ref, o_ref):
        o_ref[0] = w_ref[...].astype(BF16)

    return pl.pallas_call(
        body, name=name,
        grid_spec=pltpu.PrefetchScalarGridSpec(
            num_scalar_prefetch=1, grid=(rows // tr,),
            in_specs=[pl.BlockSpec((tr, cols), lambda i, chip_ref: (i, 0))],
            out_specs=pl.BlockSpec((1, tr, cols), lambda i, chip_ref: (chip_ref[0], i, 0))),
        out_shape=jax.ShapeDtypeStruct((N_CHIP, rows, cols), BF16),
        compiler_params=_cparams(dimension_semantics=("parallel",)),
    )(chip_arr, w)


def _halved_by_rows(shape):
    return (shape[1] // 2) % 16 == 0


def _half_of(ref, pc, block=None):
    lead = slice(None) if block is None else block
    if _halved_by_rows(ref.shape):
        h = ref.shape[1] // 2
        return ref.at[lead, pl.ds(pl.multiple_of(pc * h, 16), h), :]
    h = ref.shape[2] // 2
    return ref.at[lead, :, pl.ds(pl.multiple_of(pc * h, 128), h)]


def _half_shape(shape):
    return (shape[0], shape[1] // 2, shape[2]) if _halved_by_rows(shape) else (shape[0], shape[1], shape[2] // 2)


def _half_rows(ref, chip_id, pc):
    return _half_of(ref, pc, chip_id)


def _gather_blocks(lands, name, after=()):
    n = len(lands)
    n_in = n + len(after)

    def body(*refs):
        outs = refs[n_in:n_in + n]
        send_sems, recv_sems = refs[n_in + n:]
        x, y, c, chips = _place()
        me_chip = 2 * x + y
        sibling = (x, y, 1 - c)

        def copy(k, j, chip_id, pc, to):
            return pltpu.make_async_remote_copy(
                src_ref=_half_rows(outs[k], chip_id, pc), dst_ref=_half_rows(outs[k], chip_id, pc),
                send_sem=send_sems.at[k, j], recv_sem=recv_sems.at[k, j], device_id=to, device_id_type=MESH)

        started = []
        for k in range(n):
            for j, chip in enumerate(chips):
                cp = copy(k, j, me_chip, c, (*chip, c))
                cp.start()
                started.append(cp)
        for k in range(n):
            for j, (px, py) in enumerate(chips):
                copy(k, j, 2 * px + py, c, sibling).wait_recv()
                cp = copy(k, 3 + j, 2 * px + py, c, sibling)
                cp.start()
                started.append(cp)
        for k in range(n):
            for j, (px, py) in enumerate(chips):
                copy(k, 3 + j, 2 * px + py, 1 - c, sibling).wait_recv()
        for cp in started:
            cp.wait_send()

    return pl.pallas_call(
        body, name=name,
        out_shape=[jax.ShapeDtypeStruct(l.shape, l.dtype) for l in lands],
        in_specs=[ANY_SPEC] * n_in, out_specs=[ANY_SPEC] * n,
        input_output_aliases={i: i for i in range(n)},
        scratch_shapes=[pltpu.SemaphoreType.DMA((n, 6)), pltpu.SemaphoreType.DMA((n, 6))],
    )(*lands, *after)


def _hbm(a):
    return pltpu.with_memory_space_constraint(a, pltpu.HBM)


def _blocks_start(lands, name, after=()):
    n = len(lands)
    n_sem = 3 * n
    first = n + len(after)

    def body(*refs):
        lnd = refs[:n]
        send_sems, recv_sems = refs[first:first + n_sem], refs[first + n_sem:first + 2 * n_sem]
        token = refs[-1]
        x, y, c, chips = _place()
        me_chip = 2 * x + y
        for k in range(n):
            for j, chip in enumerate(chips):
                pltpu.make_async_remote_copy(
                    src_ref=_half_rows(lnd[k], me_chip, c), dst_ref=_half_rows(lnd[k], me_chip, c),
                    send_sem=send_sems[3 * k + j], recv_sem=recv_sems[3 * k + j],
                    device_id=(*chip, c), device_id_type=MESH).start()
        token[...] = jnp.zeros_like(token)

    out = pl.pallas_call(
        body, name=name,
        out_shape=(*[pltpu.SemaphoreType.DMA(())] * (2 * n_sem),
                   *[pltpu.HBM(l.shape, l.dtype) for l in lands],
                   jax.ShapeDtypeStruct((8, 128), F32)),
        in_specs=[HBM_SPEC] * n + [ANY_SPEC] * len(after),
        out_specs=(*[SEM_SPEC] * (2 * n_sem), *[HBM_SPEC] * n, VMEM_SPEC),
        input_output_aliases={i: 2 * n_sem + i for i in range(n)},
        compiler_params=pltpu.CompilerParams(has_side_effects=EFFECT),
    )(*[_hbm(l) for l in lands], *after)
    return list(out[:2 * n_sem]), list(out[2 * n_sem:2 * n_sem + n]), out[-1]


def _blocks_wait(sems, lands, after, name):
    n = len(lands)
    n_sem = 3 * n

    def body(*refs):
        lnd = refs[:n]
        s_sems, r_sems = refs[n:n + n_sem], refs[n + n_sem:n + 2 * n_sem]
        x, y, c, chips = _place()
        me_chip = 2 * x + y
        for k in range(n):
            for j, (px, py) in enumerate(chips):
                cp = pltpu.make_async_remote_copy(
                    src_ref=_half_rows(lnd[k], me_chip, c), dst_ref=_half_rows(lnd[k], 2 * px + py, c),
                    send_sem=s_sems[3 * k + j], recv_sem=r_sems[3 * k + j],
                    device_id=(px, py, c), device_id_type=MESH)
                cp.wait_send()
                cp.wait_recv()

    out = pl.pallas_call(
        body, name=name,
        out_shape=tuple(pltpu.HBM(l.shape, l.dtype) for l in lands),
        in_specs=[HBM_SPEC] * n + [SEM_SPEC] * (2 * n_sem) + [ANY_SPEC],
        out_specs=[HBM_SPEC] * n,
        input_output_aliases={i: i for i in range(n)},
        compiler_params=pltpu.CompilerParams(has_side_effects=EFFECT),
    )(*lands, *sems, after)
    return list(out)


def _blocks_finish(lands, name):
    n = len(lands)

    def body(*refs):
        lnd = refs[n:2 * n]
        send_sems, recv_sems = refs[2 * n:]
        x, y, c, chips = _place()
        sibling = (x, y, 1 - c)

        def copy(k, j, chip_id, pc):
            return pltpu.make_async_remote_copy(
                src_ref=_half_rows(lnd[k], chip_id, pc), dst_ref=_half_rows(lnd[k], chip_id, pc),
                send_sem=send_sems.at[k, j], recv_sem=recv_sems.at[k, j], device_id=sibling, device_id_type=MESH)

        started = []
        for k in range(n):
            for j, (px, py) in enumerate(chips):
                cp = copy(k, j, 2 * px + py, c)
                cp.start()
                started.append(cp)
        for k in range(n):
            for j, (px, py) in enumerate(chips):
                copy(k, j, 2 * px + py, 1 - c).wait_recv()
        for cp in started:
            cp.wait_send()

    out = pl.pallas_call(
        body, name=name,
        out_shape=[jax.ShapeDtypeStruct(l.shape, l.dtype) for l in lands],
        in_specs=[ANY_SPEC] * n, out_specs=[ANY_SPEC] * n,
        input_output_aliases={i: i for i in range(n)},
        scratch_shapes=[pltpu.SemaphoreType.DMA((n, 3)), pltpu.SemaphoreType.DMA((n, 3))],
    )(*lands)
    return list(out)


def _gather_start(shards, name):
    n = len(shards)
    n_sem = 3 * n

    def body(*refs):
        ins, lands = refs[:n], refs[n:2 * n]
        send_sems, recv_sems = refs[2 * n:2 * n + n_sem], refs[2 * n + n_sem:2 * n + 2 * n_sem]
        token = refs[-1]
        x, y, c, chips = _place()
        me_chip = 2 * x + y
        for k in range(n):
            h = shards[k].shape[0] // 2
            rows = pl.ds(pl.multiple_of(c * h, 8), h)
            for j, chip in enumerate(chips):
                pltpu.make_async_remote_copy(
                    src_ref=ins[k].at[rows, :], dst_ref=lands[k].at[me_chip, rows, :],
                    send_sem=send_sems[3 * k + j], recv_sem=recv_sems[3 * k + j],
                    device_id=(*chip, c), device_id_type=MESH).start()
        token[...] = jnp.zeros_like(token)

    lands = [_hbm(lax.empty((N_CHIP,) + s.shape, s.dtype)) for s in shards]
    out = pl.pallas_call(
        body, name=name,
        out_shape=(*[pltpu.SemaphoreType.DMA(())] * (2 * n_sem),
                   *[pltpu.HBM(s.shape, s.dtype) for s in shards],
                   *[pltpu.HBM(l.shape, l.dtype) for l in lands],
                   jax.ShapeDtypeStruct((8, 128), F32)),
        in_specs=[HBM_SPEC] * (2 * n),
        out_specs=(*[SEM_SPEC] * (2 * n_sem), *[HBM_SPEC] * (2 * n), VMEM_SPEC),
        input_output_aliases={i: 2 * n_sem + i for i in range(2 * n)},
        compiler_params=pltpu.CompilerParams(has_side_effects=EFFECT),
    )(*[_hbm(s) for s in shards], *lands)
    sems = list(out[:2 * n_sem])
    return sems, list(out[2 * n_sem:2 * n_sem + n]), list(out[2 * n_sem + n:2 * n_sem + 2 * n]), out[-1]


def _gather_wait(sems, shards, lands, after, name):
    n = len(shards)
    n_sem = 3 * n

    def body(*refs):
        ins, lnd = refs[:n], refs[n:2 * n]
        s_sems, r_sems = refs[2 * n:2 * n + n_sem], refs[2 * n + n_sem:2 * n + 2 * n_sem]
        x, y, c, chips = _place()
        for k in range(n):
            h = shards[k].shape[0] // 2
            rows = pl.ds(pl.multiple_of(c * h, 8), h)
            for j, (px, py) in enumerate(chips):
                cp = pltpu.make_async_remote_copy(
                    src_ref=ins[k].at[rows, :], dst_ref=lnd[k].at[2 * px + py, rows, :],
                    send_sem=s_sems[3 * k + j], recv_sem=r_sems[3 * k + j],
                    device_id=(px, py, c), device_id_type=MESH)
                cp.wait_send()
                cp.wait_recv()

    out = pl.pallas_call(
        body, name=name,
        out_shape=(*[pltpu.HBM(s.shape, s.dtype) for s in shards], *[pltpu.HBM(l.shape, l.dtype) for l in lands]),
        in_specs=[HBM_SPEC] * (2 * n) + [SEM_SPEC] * (2 * n_sem) + [ANY_SPEC],
        out_specs=[HBM_SPEC] * (2 * n),
        input_output_aliases={i: i for i in range(2 * n)},
        compiler_params=pltpu.CompilerParams(has_side_effects=EFFECT),
    )(*shards, *lands, *sems, after)
    return list(out[:n]), list(out[n:])


def _gather_finish(shards, lands, name):
    n = len(shards)

    def body(*refs):
        ins, lnd = refs[:n], refs[2 * n:3 * n]
        send_sems, recv_sems, local_sems = refs[3 * n:]
        x, y, c, chips = _place()
        me_chip = 2 * x + y
        sibling = (x, y, 1 - c)

        def half(k, chip_id, pc):
            h = shards[k].shape[0] // 2
            return lnd[k].at[chip_id, pl.ds(pl.multiple_of(pc * h, 8), h), :]

        def copy(k, j, chip_id, pc):
            return pltpu.make_async_remote_copy(
                src_ref=half(k, chip_id, pc), dst_ref=half(k, chip_id, pc),
                send_sem=send_sems.at[k, j], recv_sem=recv_sems.at[k, j], device_id=sibling, device_id_type=MESH)

        locals_, started = [], []
        for k in range(n):
            cp = pltpu.make_async_copy(ins[k], lnd[k].at[me_chip], local_sems.at[k])
            cp.start()
            locals_.append(cp)
            for j, (px, py) in enumerate(chips):
                cp = copy(k, j, 2 * px + py, c)
                cp.start()
                started.append(cp)
        for k in range(n):
            for j, (px, py) in enumerate(chips):
                copy(k, j, 2 * px + py, 1 - c).wait_recv()
        for cp in started:
            cp.wait_send()
        for cp in locals_:
            cp.wait()

    out = pl.pallas_call(
        body, name=name,
        out_shape=[jax.ShapeDtypeStruct(l.shape, l.dtype) for l in lands],
        in_specs=[ANY_SPEC] * (2 * n), out_specs=[ANY_SPEC] * n,
        input_output_aliases={n + i: i for i in range(n)},
        scratch_shapes=[pltpu.SemaphoreType.DMA((n, 3)), pltpu.SemaphoreType.DMA((n, 3)),
                        pltpu.SemaphoreType.DMA((n,))],
    )(*shards, *lands)
    return list(out)


def _send_other_half(arrs, name):
    n = len(arrs)

    def body(*refs):
        ins, outs = refs[:n], refs[n:2 * n]
        send_sems, recv_sems = refs[2 * n:]
        x, y, c, _ = _place()
        cps = []
        for k in range(n):
            cp = pltpu.make_async_remote_copy(
                src_ref=_half_of(ins[k], 1 - c), dst_ref=outs[k],
                send_sem=send_sems.at[k], recv_sem=recv_sems.at[k], device_id=(x, y, 1 - c), device_id_type=MESH)
            cp.start()
            cps.append(cp)
        for cp in cps:
            cp.wait()

    return pl.pallas_call(
        body, name=name,
        out_shape=[jax.ShapeDtypeStruct(_half_shape(a.shape), a.dtype) for a in arrs],
        in_specs=[ANY_SPEC] * n, out_specs=[ANY_SPEC] * n,
        scratch_shapes=[pltpu.SemaphoreType.DMA((n,)), pltpu.SemaphoreType.DMA((n,))],
    )(*arrs)


def _blocks_to_owner(arrs, name):
    n = len(arrs)

    def body(*refs):
        ins, outs = refs[:n], refs[n:2 * n]
        send_sems, recv_sems, local_sems = refs[2 * n:]
        x, y, c, chips = _place()
        me_chip = 2 * x + y
        locals_, started = [], []
        for k in range(n):
            cp = pltpu.make_async_copy(ins[k].at[me_chip], outs[k].at[me_chip], local_sems.at[k])
            cp.start()
            locals_.append(cp)

        def copy(k, j, src_block, dst_slot, to):
            return pltpu.make_async_remote_copy(
                src_ref=ins[k].at[src_block], dst_ref=outs[k].at[dst_slot],
                send_sem=send_sems.at[k, j], recv_sem=recv_sems.at[k, j], device_id=to, device_id_type=MESH)

        for k in range(n):
            for j, (px, py) in enumerate(chips):
                cp = copy(k, j, 2 * px + py, me_chip, (px, py, c))
                cp.start()
                started.append(cp)
        for k in range(n):
            for j, (px, py) in enumerate(chips):
                copy(k, j, me_chip, 2 * px + py, (px, py, c)).wait_recv()
        for cp in started:
            cp.wait_send()
        for cp in locals_:
            cp.wait()

    return pl.pallas_call(
        body, name=name,
        out_shape=[jax.ShapeDtypeStruct(a.shape, a.dtype) for a in arrs],
        in_specs=[ANY_SPEC] * n, out_specs=[ANY_SPEC] * n,
        scratch_shapes=[pltpu.SemaphoreType.DMA((n, 3)), pltpu.SemaphoreType.DMA((n, 3)),
                        pltpu.SemaphoreType.DMA((n,))],
    )(*arrs)


def _scatter_blocks(arrs, name):
    n = len(arrs)

    def body(*refs):
        ins, outs = refs[:n], refs[n:2 * n]
        send_sems, recv_sems = refs[2 * n:]
        x, y, c, chips = _place()
        me_chip = 2 * x + y

        def copy(k, j, src_block, dst_slot, to):
            return pltpu.make_async_remote_copy(
                src_ref=ins[k].at[src_block], dst_ref=outs[k].at[dst_slot],
                send_sem=send_sems.at[k, j], recv_sem=recv_sems.at[k, j], device_id=to, device_id_type=MESH)

        started = []
        for k in range(n):
            for j, (px, py) in enumerate(chips):
                cp = copy(k, j, 2 * px + py, me_chip, (px, py, c))
                cp.start()
                started.append(cp)
        for k in range(n):
            for j, (px, py) in enumerate(chips):
                copy(k, j, me_chip, 2 * px + py, (px, py, c)).wait_recv()
        for cp in started:
            cp.wait_send()

    return pl.pallas_call(
        body, name=name,
        out_shape=[jax.ShapeDtypeStruct(a.shape, a.dtype) for a in arrs],
        in_specs=[ANY_SPEC] * n, out_specs=[ANY_SPEC] * n,
        scratch_shapes=[pltpu.SemaphoreType.DMA((n, 3)), pltpu.SemaphoreType.DMA((n, 3))],
    )(*arrs)


def _scatter_start(arrs, name, after=()):
    n = len(arrs)
    n_sem = 3 * n
    first = 2 * n + len(after)

    def body(*refs):
        ins, lnd = refs[:n], refs[n:2 * n]
        send_sems, recv_sems = refs[first:first + n_sem], refs[first + n_sem:first + 2 * n_sem]
        token = refs[-1]
        x, y, c, chips = _place()
        me_chip = 2 * x + y
        for k in range(n):
            for j, (px, py) in enumerate(chips):
                pltpu.make_async_remote_copy(
                    src_ref=ins[k].at[2 * px + py], dst_ref=lnd[k].at[me_chip],
                    send_sem=send_sems[3 * k + j], recv_sem=recv_sems[3 * k + j],
                    device_id=(px, py, c), device_id_type=MESH).start()
        token[...] = jnp.zeros_like(token)

    lands = [_hbm(lax.empty(a.shape, a.dtype)) for a in arrs]
    out = pl.pallas_call(
        body, name=name,
        out_shape=(*[pltpu.SemaphoreType.DMA(())] * (2 * n_sem),
                   *[pltpu.HBM(a.shape, a.dtype) for a in arrs], *[pltpu.HBM(a.shape, a.dtype) for a in arrs],
                   jax.ShapeDtypeStruct((8, 128), F32)),
        in_specs=[HBM_SPEC] * (2 * n) + [ANY_SPEC] * len(after),
        out_specs=(*[SEM_SPEC] * (2 * n_sem), *[HBM_SPEC] * (2 * n), VMEM_SPEC),
        input_output_aliases={i: 2 * n_sem + i for i in range(2 * n)},
        compiler_params=pltpu.CompilerParams(has_side_effects=EFFECT),
    )(*[_hbm(a) for a in arrs], *lands, *after)
    base = 2 * n_sem
    return list(out[:base]), list(out[base:base + n]), list(out[base + n:base + 2 * n]), out[-1]


def _scatter_wait(sems, arrs, lands, after, name):
    n = len(arrs)
    n_sem = 3 * n

    def body(*refs):
        ins, lnd = refs[:n], refs[n:2 * n]
        s_sems, r_sems = refs[2 * n:2 * n + n_sem], refs[2 * n + n_sem:2 * n + 2 * n_sem]
        x, y, c, chips = _place()
        for k in range(n):
            for j, (px, py) in enumerate(chips):
                cp = pltpu.make_async_remote_copy(
                    src_ref=ins[k].at[2 * px + py], dst_ref=lnd[k].at[2 * px + py],
                    send_sem=s_sems[3 * k + j], recv_sem=r_sems[3 * k + j],
                    device_id=(px, py, c), device_id_type=MESH)
                cp.wait_send()
                cp.wait_recv()

    out = pl.pallas_call(
        body, name=name,
        out_shape=tuple(pltpu.HBM(a.shape, a.dtype) for a in list(arrs) + list(lands)),
        in_specs=[HBM_SPEC] * (2 * n) + [SEM_SPEC] * (2 * n_sem) + [ANY_SPEC] * len(after),
        out_specs=[HBM_SPEC] * (2 * n),
        input_output_aliases={i: i for i in range(2 * n)},
        compiler_params=pltpu.CompilerParams(has_side_effects=EFFECT),
    )(*arrs, *lands, *sems, *after)
    return list(out[:n]), list(out[n:])


def _sum_owner(chip_arr, pairs, got, name):
    nb, h, cols = got.shape
    tr = _row_tile(h, 16, 256)

    def body(chip_ref, own_ref, a_ref, b_ref, c_ref, o_ref):
        o_ref[...] = ((own_ref[0].astype(F32) + a_ref[0].astype(F32)) + b_ref[0].astype(F32)) + c_ref[0].astype(F32)

    def slot(off):
        return pl.BlockSpec((1, tr, cols), lambda i, chip_ref: ((chip_ref[0] + off) % N_CHIP, i, 0))

    return pl.pallas_call(
        body, name=name,
        grid_spec=pltpu.PrefetchScalarGridSpec(
            num_scalar_prefetch=1, grid=(h // tr,),
            in_specs=[slot(0), slot(1), slot(2), slot(3)],
            out_specs=pl.BlockSpec((tr, cols), lambda i, chip_ref: (i, 0))),
        out_shape=jax.ShapeDtypeStruct((h, cols), F32),
        compiler_params=_cparams(dimension_semantics=("parallel",)),
    )(chip_arr, pairs, got, got, got)


def _swap_with_sibling(arrs, name):
    n = len(arrs)

    def body(*refs):
        ins, outs = refs[:n], refs[n:2 * n]
        send_sems, recv_sems = refs[2 * n:]
        x, y, c, _ = _place()
        cps = []
        for k in range(n):
            cp = pltpu.make_async_remote_copy(
                src_ref=ins[k], dst_ref=outs[k], send_sem=send_sems.at[k], recv_sem=recv_sems.at[k],
                device_id=(x, y, 1 - c), device_id_type=MESH)
            cp.start()
            cps.append(cp)
        for cp in cps:
            cp.wait()

    return pl.pallas_call(
        body, name=name,
        out_shape=[jax.ShapeDtypeStruct(a.shape, a.dtype) for a in arrs],
        in_specs=[ANY_SPEC] * n, out_specs=[ANY_SPEC] * n,
        scratch_shapes=[pltpu.SemaphoreType.DMA((n,)), pltpu.SemaphoreType.DMA((n,))],
    )(*arrs)


def _row_tile(h, mult=8, cap=128):
    for t in range(cap - cap % mult, mult - 1, -mult):
        if h % t == 0:
            return t
    if mult > 8:
        return _row_tile(h, 8, cap)
    raise ValueError(h)


def _cast_bf16(a, name):
    rows, cols = a.shape
    tr = _row_tile(rows, 16, 256)

    def body(a_ref, o_ref):
        o_ref[...] = a_ref[...].astype(BF16)

    return pl.pallas_call(
        body, name=name, grid=(rows // tr,),
        out_shape=jax.ShapeDtypeStruct(a.shape, BF16),
        in_specs=[pl.BlockSpec((tr, cols), lambda i: (i, 0))],
        out_specs=pl.BlockSpec((tr, cols), lambda i: (i, 0)),
        compiler_params=_cparams(dimension_semantics=("parallel",)),
    )(a)


def _pair_sum(c_arr, full, recv, name):
    nb, rows, cols = full.shape

    def body(c_ref, f_ref, r_ref, o_ref):
        o_ref[...] = (f_ref[...] + r_ref[...]).astype(BF16)

    if _halved_by_rows(full.shape):
        h = rows // 2
        tr = _row_tile(h, 16, 256)
        steps = h // tr
        own = pl.BlockSpec((1, tr, cols), lambda b, i, c_ref: (b, c_ref[0] * steps + i, 0))
        half = pl.BlockSpec((1, tr, cols), lambda b, i, c_ref: (b, i, 0))
    else:
        steps = 1
        own = pl.BlockSpec((1, rows, cols // 2), lambda b, i, c_ref: (b, 0, c_ref[0]))
        half = pl.BlockSpec((1, rows, cols // 2), lambda b, i, c_ref: (b, 0, 0))
    return pl.pallas_call(
        body, name=name,
        grid_spec=pltpu.PrefetchScalarGridSpec(
            num_scalar_prefetch=1, grid=(nb, steps), in_specs=[own, half], out_specs=half),
        out_shape=jax.ShapeDtypeStruct(_half_shape(full.shape), BF16),
        compiler_params=_cparams(dimension_semantics=("parallel", "parallel")),
    )(c_arr, full, recv)


def _sum_chips(got, name):
    nb, h, cols = got.shape
    tr = _row_tile(h, 16, 256)

    def body(g_ref, o_ref):
        g = g_ref[...].astype(F32)
        o_ref[...] = ((g[0] + g[1]) + g[2]) + g[3]

    return pl.pallas_call(
        body, name=name, grid=(h // tr,),
        out_shape=jax.ShapeDtypeStruct((h, cols), F32),
        in_specs=[pl.BlockSpec((nb, tr, cols), lambda i: (0, i, 0))],
        out_specs=pl.BlockSpec((tr, cols), lambda i: (i, 0)),
        compiler_params=_cparams(dimension_semantics=("parallel",)),
    )(got)


def _adam_math(g, w, m, v):
    m1 = ADAM_B1 * m + (1.0 - ADAM_B1) * g
    v1 = ADAM_B2 * v + (1.0 - ADAM_B2) * (g * g)
    m_hat = m1 / (1.0 - ADAM_B1 ** ADAM_STEP)
    v_hat = v1 / (1.0 - ADAM_B2 ** ADAM_STEP)
    delta = -ADAM_LR * (m_hat / (jnp.sqrt(v_hat) + ADAM_EPS) + ADAM_WD * w)
    return delta, m1, v1


def _adamw_halves(c_arr, own, other, w, m, v, name):
    rows, cols = w.shape

    def body(c_ref, own_ref, oth_ref, w_ref, m_ref, v_ref, g_out, d_out, m_out, v_out):
        g = jnp.where(pl.program_id(0) == c_ref[0], own_ref[...], oth_ref[...])
        d, m1, v1 = _adam_math(g, w_ref[...], m_ref[...], v_ref[...])
        g_out[...] = g
        d_out[...] = d
        m_out[...] = m1
        v_out[...] = v1

    if own.shape[1] == cols:
        h = rows // 2
        tr = _row_tile(h)
        steps = h // tr
        half_spec = pl.BlockSpec((tr, cols), lambda p, i, c_ref: (i, 0))
        full_spec = pl.BlockSpec((tr, cols), lambda p, i, c_ref: (p * steps + i, 0))
    else:
        tr = _row_tile(rows)
        steps = rows // tr
        half_spec = pl.BlockSpec((tr, cols // 2), lambda p, i, c_ref: (i, 0))
        full_spec = pl.BlockSpec((tr, cols // 2), lambda p, i, c_ref: (i, p))
    return pl.pallas_call(
        body, name=name,
        grid_spec=pltpu.PrefetchScalarGridSpec(
            num_scalar_prefetch=1, grid=(2, steps),
            in_specs=[half_spec, half_spec, full_spec, full_spec, full_spec],
            out_specs=[full_spec] * 4),
        out_shape=[jax.ShapeDtypeStruct(w.shape, F32)] * 4,
        compiler_params=_cparams(dimension_semantics=("parallel", "parallel")),
    )(c_arr, own, other, w, m, v)


def _adamw_whole(items, name):
    n = len(items)

    def body(*refs):
        ins, outs = refs[:4 * n], refs[4 * n:]
        for k in range(n):
            g, w, m, v = (r[...] for r in ins[4 * k:4 * k + 4])
            d, m1, v1 = _adam_math(g, w, m, v)
            outs[3 * k][...] = d
            outs[3 * k + 1][...] = m1
            outs[3 * k + 2][...] = v1

    flat = [a for it in items for a in it]
    shapes = [jax.ShapeDtypeStruct(it[1].shape, F32) for it in items for _ in range(3)]
    out = pl.pallas_call(
        body, name=name, out_shape=shapes,
        in_specs=[VMEM_SPEC] * (4 * n), out_specs=[VMEM_SPEC] * (3 * n),
        compiler_params=_cparams(),
    )(*flat)
    return [tuple(out[3 * k:3 * k + 3]) for k in range(n)]


def _adamw_tiled(g, w, m, v, name):
    rows, cols = w.shape
    tr = _row_tile(rows)

    def body(g_ref, w_ref, m_ref, v_ref, d_out, m_out, v_out):
        d, m1, v1 = _adam_math(g_ref[...], w_ref[...], m_ref[...], v_ref[...])
        d_out[...] = d
        m_out[...] = m1
        v_out[...] = v1

    spec = pl.BlockSpec((tr, cols), lambda i: (i, 0))
    return pl.pallas_call(
        body, name=name, grid=(rows // tr,),
        out_shape=[jax.ShapeDtypeStruct(w.shape, F32)] * 3,
        in_specs=[spec] * 4, out_specs=[spec] * 3,
        compiler_params=_cparams(dimension_semantics=("parallel",)),
    )(g, w, m, v)


def _mod_forward(cond, w_mod, b_mod_cols, name):
    def body(c_ref, w_ref, b_ref, o_ref):
        o_ref[...] = _dot(_silu(c_ref[...]), w_ref[...]) + b_ref[...]

    return pl.pallas_call(
        body, name=name, out_shape=jax.ShapeDtypeStruct((cond.shape[0], w_mod.shape[1]), F32),
        in_specs=[VMEM_SPEC] * 3, out_specs=VMEM_SPEC, compiler_params=_cparams(),
    )(cond, w_mod, b_mod_cols)


def _mod_backward(cond, w_mod, dmod_cols, name):
    def body(c_ref, w_ref, d_ref, gw_ref, gc_ref):
        s = _silu(c_ref[...])
        d = d_ref[...]
        gw_ref[...] = _dot_tn(s, d)
        gc_ref[...] = _dot_nt(d[8:16, :], w_ref[...])

    return pl.pallas_call(
        body, name=name,
        out_shape=[jax.ShapeDtypeStruct(w_mod.shape, F32), jax.ShapeDtypeStruct((8, w_mod.shape[0]), F32)],
        in_specs=[VMEM_SPEC] * 3, out_specs=[VMEM_SPEC] * 2, compiler_params=_cparams(),
    )(cond, w_mod, dmod_cols)


def _col_chunks(width, step=512):
    return [(s, min(step, width - s)) for s in range(0, width, step)]


def _w_in_row(p_off):
    if p_off < 9 * HW:
        return p_off
    return 9 * HW if p_off == OFF_LR else p_off + 2 * RANK


def _in_projection(ctx0, x0, modc, modx, pre1, w_t, n_ctx_tiles, name):
    d = x0.shape[1]
    rows = ctx0.shape[0] + x0.shape[0]
    width = P_WIDTH

    def body(ctx_ref, x_ref, modc_ref, modx_ref, pre_ref, w_ref, h_ref, p_ref):
        is_ctx = pl.program_id(0) < n_ctx_tiles
        n, _ = _rms(jnp.where(is_ctx, ctx_ref[...], x_ref[...]))
        shift = jnp.where(is_ctx, modc_ref[0:1, :], modx_ref[0:1, :])
        scale = jnp.where(is_ctx, modc_ref[1:2, :], modx_ref[1:2, :])
        h = (n * pre_ref[...] * (1.0 + scale) + shift).astype(BF16)
        h_ref[...] = h
        for s, w in _col_chunks(width):
            p_ref[:, s:s + w] = _dot_nt(h, w_ref[_w_in_row(s):_w_in_row(s) + w, :])

    row = lambda i: (i, 0)
    fixed = lambda i: (0, 0)
    return pl.pallas_call(
        body, name=name, grid=(rows // TM,),
        out_shape=[jax.ShapeDtypeStruct((rows, d), BF16), jax.ShapeDtypeStruct((rows, width), F32)],
        in_specs=[pl.BlockSpec((TM, d), lambda i: (jnp.minimum(i, n_ctx_tiles - 1), 0)),
                  pl.BlockSpec((TM, d), lambda i: (jnp.maximum(i - n_ctx_tiles, 0), 0)),
                  pl.BlockSpec((8, d), fixed), pl.BlockSpec((8, d), fixed), pl.BlockSpec((1, d), fixed), VMEM_SPEC],
        out_specs=[pl.BlockSpec((TM, d), row), pl.BlockSpec((TM, width), row)],
        compiler_params=_cparams(dimension_semantics=("parallel",)),
    )(ctx0, x0, modc, modx, pre1, w_t)


C_HQ, C_HI, C_HF_FW, C_HF_BW, C_HGATE, C_GQ, C_GK, C_GV, C_GGATE = range(9)
OFF_GATE_HG = 9 * HW
OFF_LR = 13 * HW
P_WIDTH = OFF_LR + 128


def _head_norm_fwd(o, w):
    outs, ns, rs = [], [], []
    for h in range(NH):
        n, r = _rms(o[:, h * HD:(h + 1) * HD])
        ns.append(n)
        rs.append(r)
        outs.append(n * w)
    return jnp.concatenate(outs, axis=1), ns, rs


def _mixer_tail(z, o_hg, o_gla, p_hgate, p_ggate, p_gate_hg, p_gate_gla, hg_on, gla_on, wbh, wbg, wout):
    on_hg, n_hg, r_hg = _head_norm_fwd(o_hg, hg_on)
    on_gla, n_gla, r_gla = _head_norm_fwd(o_gla, gla_on)
    og_hg = (on_hg * _silu(p_hgate)).astype(BF16)
    og_gla = (on_gla * _silu(p_ggate)).astype(BF16)
    b_hg = jnp.dot(og_hg, wbh, preferred_element_type=F32)
    b_gla = jnp.dot(og_gla, wbg, preferred_element_type=F32)
    s_hg = _sigmoid(p_gate_hg)
    s_gla = _sigmoid(p_gate_gla)
    merged = (s_hg * b_hg + s_gla * b_gla).astype(BF16)
    y1 = jnp.dot(merged, wout, preferred_element_type=F32)
    return dict(on_hg=on_hg, n_hg=n_hg, r_hg=r_hg, on_gla=on_gla, n_gla=n_gla, r_gla=r_gla, og_hg=og_hg,
                og_gla=og_gla, b_hg=b_hg, b_gla=b_gla, s_hg=s_hg, s_gla=s_gla, merged=merged, y1=y1)


def _mixer_ffn(x_lat, p, o_list, modx, norms, onorms, w_br_hg, w_br_gla, w_out, w_gate, w_up, w_down, target,
               n_ctx_tiles, name):
    rows, d = x_lat.shape
    dff = w_gate.shape[0]
    inv_d = 1.0 / d

    def body(x_ref, ofw_hg, obw_hg, ofw_gla, obw_gla, p_hgate, p_ggate, p_ghg_a, p_ghg_b, p_ggla_a, p_ggla_b,
             modx_ref, norm_ref, on_ref, wbh_ref, wbg_ref, wout_ref, wg_ref, wu_ref, wd_ref, t_ref,
             loss_ref, dz2_ref, y1_ref, mrg_ref, oghg_ref, oggla_ref, h2_ref, a_ref, du_ref, dv_ref, dy2_ref,
             stat_ref):
        i = pl.program_id(0)
        post1, pre2, post2 = norm_ref[1:2, :], norm_ref[2:3, :], norm_ref[3:4, :]
        gate1, shift2, scale2, gate2 = modx_ref[2:3, :], modx_ref[3:4, :], modx_ref[4:5, :], modx_ref[5:6, :]
        p_gate_hg = jnp.concatenate([p_ghg_a[...], p_ghg_b[...]], axis=1)
        p_gate_gla = jnp.concatenate([p_ggla_a[...], p_ggla_b[...]], axis=1)
        t = _mixer_tail(x_ref[...], ofw_hg[...] + obw_hg[...], ofw_gla[...] + obw_gla[...], p_hgate[...],
                        p_ggate[...], p_gate_hg, p_gate_gla, on_ref[0:1, 0:HD], on_ref[1:2, 0:HD],
                        wbh_ref[...], wbg_ref[...], wout_ref[...])
        y1_ref[...] = t["y1"]
        mrg_ref[...] = t["merged"]
        oghg_ref[...] = t["og_hg"]
        oggla_ref[...] = t["og_gla"]
        n1, _ = _rms(t["y1"])
        z2 = x_ref[...] + n1 * post1 * gate1
        n2, r2 = _rms(z2)
        nw2 = n2 * pre2
        h2 = (nw2 * (1.0 + scale2) + shift2).astype(BF16)
        h2_ref[...] = h2
        u = _dot_nt(h2, wg_ref[...])
        v = _dot_nt(h2, wu_ref[...])
        su = _silu(u)
        a = (su * v).astype(BF16)
        a_ref[...] = a
        y2 = jnp.dot(a, wd_ref[...], preferred_element_type=F32)
        n3, r3 = _rms(y2)
        z3 = z2 + n3 * post2 * gate2
        err = z3 - t_ref[...]
        part = 0.5 * inv_d * jnp.sum(err * err)
        dz3 = err * inv_d
        dgate2 = _colsum(dz3 * n3 * post2)
        tt = dz3 * gate2
        dpost2 = _colsum(tt * n3)
        dy2 = _rms_bwd(tt * post2, n3, r3).astype(BF16)
        dy2_ref[...] = dy2
        da = _dot_nt(dy2, wd_ref[...])
        du = (da * v * _dsilu(u)).astype(BF16)
        dv = (da * su).astype(BF16)
        du_ref[...] = du
        dv_ref[...] = dv
        dh2 = (jnp.dot(du, wg_ref[...], preferred_element_type=F32)
               + jnp.dot(dv, wu_ref[...], preferred_element_type=F32))
        dshift2 = _colsum(dh2)
        dscale2 = _colsum(dh2 * nw2)
        dnw2 = dh2 * (1.0 + scale2)
        dpre2 = _colsum(dnw2 * n2)
        dz2_ref[...] = dz3 + _rms_bwd(dnw2 * pre2, n2, r2)

        @pl.when(i == 0)
        def _():
            stat_ref[...] = jnp.zeros_like(stat_ref)
            loss_ref[...] = jnp.zeros_like(loss_ref)

        for r, val in enumerate((dshift2, dscale2, dgate2, dpre2, dpost2)):
            stat_ref[r:r + 1, :] += val
        loss_ref[...] += part

    tm = TM_FFN
    ctx_tiles = n_ctx_tiles * (TM // tm)
    lat = lambda i: (i, 0)
    full = lambda i: (i + ctx_tiles, 0)
    fixed = lambda i: (0, 0)

    def pcol(blk):
        return pl.BlockSpec((tm, HW), lambda i: (i + ctx_tiles, blk))

    in_specs = ([pl.BlockSpec((tm, d), lat)] + [pl.BlockSpec((tm, HW), full)] * 4
                + [pcol(C_HGATE), pcol(C_GGATE), pcol(9), pcol(10), pcol(11), pcol(12)]
                + [pl.BlockSpec((8, d), fixed), pl.BlockSpec((8, d), fixed), pl.BlockSpec((8, d), fixed)]
                + [VMEM_SPEC] * 6 + [pl.BlockSpec((tm, d), lat)])
    bf = lambda w: jax.ShapeDtypeStruct((rows, w), BF16)
    out_shape = [jax.ShapeDtypeStruct((8, 128), F32), jax.ShapeDtypeStruct((rows, d), F32),
                 jax.ShapeDtypeStruct((rows, d), F32), bf(d), bf(HW), bf(HW), bf(d), bf(dff), bf(dff), bf(dff), bf(d),
                 jax.ShapeDtypeStruct((8, d), F32)]
    out_specs = [pl.BlockSpec((8, 128), fixed), pl.BlockSpec((tm, d), lat), pl.BlockSpec((tm, d), lat),
                 pl.BlockSpec((tm, d), lat), pl.BlockSpec((tm, HW), lat), pl.BlockSpec((tm, HW), lat),
                 pl.BlockSpec((tm, d), lat), pl.BlockSpec((tm, dff), lat), pl.BlockSpec((tm, dff), lat),
                 pl.BlockSpec((tm, dff), lat), pl.BlockSpec((tm, d), lat), pl.BlockSpec((8, d), fixed)]
    return pl.pallas_call(
        body, name=name, grid=(rows // tm,), out_shape=out_shape, in_specs=in_specs, out_specs=out_specs,
        compiler_params=_cparams(dimension_semantics=("arbitrary",)),
    )(x_lat, *o_list, p, p, p, p, p, p, modx, norms, onorms, w_br_hg, w_br_gla, w_out, w_gate, w_up, w_down, target)


def _mixer_tail_fwd(x_lat, p, o_list, modx, norms, onorms, w_br_hg, w_br_gla, w_out, n_ctx_tiles, name):
    rows, d = x_lat.shape

    def body(x_ref, ofw_hg, obw_hg, ofw_gla, obw_gla, p_hgate, p_ggate, p_ghg_a, p_ghg_b, p_ggla_a, p_ggla_b,
             modx_ref, norm_ref, on_ref, wbh_ref, wbg_ref, wout_ref, z2_ref, y1_ref, mrg_ref, oghg_ref, oggla_ref):
        p_gate_hg = jnp.concatenate([p_ghg_a[...], p_ghg_b[...]], axis=1)
        p_gate_gla = jnp.concatenate([p_ggla_a[...], p_ggla_b[...]], axis=1)
        t = _mixer_tail(x_ref[...], ofw_hg[...] + obw_hg[...], ofw_gla[...] + obw_gla[...], p_hgate[...],
                        p_ggate[...], p_gate_hg, p_gate_gla, on_ref[0:1, 0:HD], on_ref[1:2, 0:HD],
                        wbh_ref[...], wbg_ref[...], wout_ref[...])
        y1_ref[...] = t["y1"]
        mrg_ref[...] = t["merged"]
        oghg_ref[...] = t["og_hg"]
        oggla_ref[...] = t["og_gla"]
        n1, _ = _rms(t["y1"])
        z2_ref[...] = x_ref[...] + n1 * norm_ref[1:2, :] * modx_ref[2:3, :]

    lat = lambda i: (i, 0)
    full = lambda i: (i + n_ctx_tiles, 0)
    fixed = lambda i: (0, 0)

    def pcol(blk):
        return pl.BlockSpec((TM, HW), lambda i: (i + n_ctx_tiles, blk))

    in_specs = ([pl.BlockSpec((TM, d), lat)] + [pl.BlockSpec((TM, HW), full)] * 4
                + [pcol(C_HGATE), pcol(C_GGATE), pcol(9), pcol(10), pcol(11), pcol(12)]
                + [pl.BlockSpec((8, d), fixed)] * 3 + [VMEM_SPEC] * 3)
    bf = lambda w: jax.ShapeDtypeStruct((rows, w), BF16)
    f32 = jax.ShapeDtypeStruct((rows, d), F32)
    return pl.pallas_call(
        body, name=name, grid=(rows // TM,), out_shape=[f32, f32, bf(d), bf(HW), bf(HW)], in_specs=in_specs,
        out_specs=[pl.BlockSpec((TM, d), lat)] * 3 + [pl.BlockSpec((TM, HW), lat)] * 2,
        compiler_params=_cparams(dimension_semantics=("parallel",)),
    )(x_lat, *o_list, p, p, p, p, p, p, modx, norms, onorms, w_br_hg, w_br_gla, w_out)


def _ffn_fwd_bwd(z2, modx, norms, w_gate, w_up, w_down, target, name):
    rows, d = z2.shape
    dff = w_gate.shape[0]
    inv_d = 1.0 / d

    def body(z2_ref, modx_ref, norm_ref, wg_ref, wu_ref, wd_ref, t_ref,
             loss_ref, dz2_ref, h2_ref, a_ref, du_ref, dv_ref, dy2_ref, stat_ref):
        i = pl.program_id(0)
        pre2, post2 = norm_ref[2:3, :], norm_ref[3:4, :]
        shift2, scale2, gate2 = modx_ref[3:4, :], modx_ref[4:5, :], modx_ref[5:6, :]
        z2 = z2_ref[...]
        n2, r2 = _rms(z2)
        nw2 = n2 * pre2
        h2 = (nw2 * (1.0 + scale2) + shift2).astype(BF16)
        h2_ref[...] = h2
        u = _dot_nt(h2, wg_ref[...])
        v = _dot_nt(h2, wu_ref[...])
        su = _silu(u)
        a = (su * v).astype(BF16)
        a_ref[...] = a
        y2 = jnp.dot(a, wd_ref[...], preferred_element_type=F32)
        n3, r3 = _rms(y2)
        err = z2 + n3 * post2 * gate2 - t_ref[...]
        part = 0.5 * inv_d * jnp.sum(err * err)
        dz3 = err * inv_d
        dgate2 = _colsum(dz3 * n3 * post2)
        tt = dz3 * gate2
        dpost2 = _colsum(tt * n3)
        dy2 = _rms_bwd(tt * post2, n3, r3).astype(BF16)
        dy2_ref[...] = dy2
        da = _dot_nt(dy2, wd_ref[...])
        du = (da * v * _dsilu(u)).astype(BF16)
        dv = (da * su).astype(BF16)
        du_ref[...] = du
        dv_ref[...] = dv
        dh2 = (jnp.dot(du, wg_ref[...], preferred_element_type=F32)
               + jnp.dot(dv, wu_ref[...], preferred_element_type=F32))
        dshift2 = _colsum(dh2)
        dscale2 = _colsum(dh2 * nw2)
        dnw2 = dh2 * (1.0 + scale2)
        dpre2 = _colsum(dnw2 * n2)
        dz2_ref[...] = dz3 + _rms_bwd(dnw2 * pre2, n2, r2)

        @pl.when(i == 0)
        def _():
            stat_ref[...] = jnp.zeros_like(stat_ref)
            loss_ref[...] = jnp.zeros_like(loss_ref)

        for r, val in enumerate((dshift2, dscale2, dgate2, dpre2, dpost2)):
            stat_ref[r:r + 1, :] += val
        loss_ref[...] += part

    lat = lambda i: (i, 0)
    fixed = lambda i: (0, 0)
    bf = lambda w: jax.ShapeDtypeStruct((rows, w), BF16)
    return pl.pallas_call(
        body, name=name, grid=(rows // TM,),
        out_shape=[jax.ShapeDtypeStruct((8, 128), F32), jax.ShapeDtypeStruct((rows, d), F32), bf(d), bf(dff), bf(dff),
                   bf(dff), bf(d), jax.ShapeDtypeStruct((8, d), F32)],
        in_specs=[pl.BlockSpec((TM, d), lat), pl.BlockSpec((8, d), fixed), pl.BlockSpec((8, d), fixed)]
        + [VMEM_SPEC] * 3 + [pl.BlockSpec((TM, d), lat)],
        out_specs=[pl.BlockSpec((8, 128), fixed), pl.BlockSpec((TM, d), lat), pl.BlockSpec((TM, d), lat),
                   pl.BlockSpec((TM, dff), lat), pl.BlockSpec((TM, dff), lat), pl.BlockSpec((TM, dff), lat),
                   pl.BlockSpec((TM, d), lat), pl.BlockSpec((8, d), fixed)],
        compiler_params=_cparams(dimension_semantics=("arbitrary",)),
    )(z2, modx, norms, w_gate, w_up, w_down, target)


def _mixer_tail_bwd(x_lat, p, o_list, dz2, y1, modx, norms, onorms, w_br_hg, w_br_gla, w_out, n_ctx_tiles, n_tiles,
                    name):
    rows, d = x_lat.shape
    total = n_tiles * TM

    def body(x_ref, ofw_hg, obw_hg, ofw_gla, obw_gla, p_hgate, p_ggate, p_ghg_a, p_ghg_b, p_ggla_a, p_ggla_b,
             dz2_ref, y1_ref, modx_ref, norm_ref, on_ref, wbh_ref, wbg_ref, wout_ref,
             dohg_ref, dogla_ref, dhgate_ref, dggate_ref, dghg_ref, dggla_ref, dy1_ref, dbhg_ref, dbgla_ref,
             stat_ref):
        i = pl.program_id(0)

        @pl.when(i == 0)
        def _():
            stat_ref[...] = jnp.zeros_like(stat_ref)

        @pl.when(i < n_ctx_tiles)
        def _():
            for ref in (dohg_ref, dogla_ref, dhgate_ref, dggate_ref, dghg_ref, dggla_ref):
                ref[...] = jnp.zeros_like(ref)

        @pl.when(i >= n_ctx_tiles)
        def _():
            post1, gate1 = norm_ref[1:2, :], modx_ref[2:3, :]
            hg_on, gla_on = on_ref[0:1, 0:HD], on_ref[1:2, 0:HD]
            p_gate_hg = jnp.concatenate([p_ghg_a[...], p_ghg_b[...]], axis=1)
            p_gate_gla = jnp.concatenate([p_ggla_a[...], p_ggla_b[...]], axis=1)
            ph, pg = p_hgate[...], p_ggate[...]
            t = _mixer_tail(x_ref[...], ofw_hg[...] + obw_hg[...], ofw_gla[...] + obw_gla[...], ph, pg,
                            p_gate_hg, p_gate_gla, hg_on, gla_on, wbh_ref[...], wbg_ref[...], wout_ref[...])
            dz2 = dz2_ref[...]
            n1, r1 = _rms(y1_ref[...])
            dgate1 = _colsum(dz2 * n1 * post1)
            tt = dz2 * gate1
            dpost1 = _colsum(tt * n1)
            dy1 = _rms_bwd(tt * post1, n1, r1).astype(BF16)
            dy1_ref[...] = dy1
            dmerged = _dot_nt(dy1, wout_ref[...])
            dghg_ref[...] = dmerged * t["b_hg"] * t["s_hg"] * (1.0 - t["s_hg"])
            dggla_ref[...] = dmerged * t["b_gla"] * t["s_gla"] * (1.0 - t["s_gla"])
            db_hg = (dmerged * t["s_hg"]).astype(BF16)
            db_gla = (dmerged * t["s_gla"]).astype(BF16)
            dbhg_ref[...] = db_hg
            dbgla_ref[...] = db_gla
            don_acc = []
            for (db, wb, pgate, on, ns, rs, gain, gate_ref, do_ref) in (
                    (db_hg, wbh_ref, ph, t["on_hg"], t["n_hg"], t["r_hg"], hg_on, dhgate_ref, dohg_ref),
                    (db_gla, wbg_ref, pg, t["on_gla"], t["n_gla"], t["r_gla"], gla_on, dggate_ref, dogla_ref)):
                dog = _dot_nt(db, wb[...])
                gate_ref[...] = dog * on * _dsilu(pgate)
                don = dog * _silu(pgate)
                acc = jnp.zeros((1, HD), F32)
                for h in range(NH):
                    sl = slice(h * HD, (h + 1) * HD)
                    acc = acc + _colsum(don[:, sl] * ns[h])
                    do_ref[:, sl] = _rms_bwd(don[:, sl] * gain, ns[h], rs[h])
                don_acc.append(acc)
            stat_ref[0:1, :] += dgate1
            stat_ref[1:2, :] += dpost1
            stat_ref[2:3, 0:HD] += don_acc[0]
            stat_ref[2:3, HD:2 * HD] += don_acc[1]

    lat = lambda i: (jnp.maximum(i - n_ctx_tiles, 0), 0)
    full = lambda i: (i, 0)
    fixed = lambda i: (0, 0)

    def pcol(blk):
        return pl.BlockSpec((TM, HW), lambda i: (i, blk))

    in_specs = ([pl.BlockSpec((TM, d), lat)] + [pl.BlockSpec((TM, HW), full)] * 4
                + [pcol(C_HGATE), pcol(C_GGATE), pcol(9), pcol(10), pcol(11), pcol(12)]
                + [pl.BlockSpec((TM, d), lat), pl.BlockSpec((TM, d), lat)]
                + [pl.BlockSpec((8, d), fixed)] * 3 + [VMEM_SPEC] * 3)
    f = lambda w: jax.ShapeDtypeStruct((total, w), F32)
    out_shape = [f(HW), f(HW), f(HW), f(HW), f(d), f(d), jax.ShapeDtypeStruct((rows, d), BF16),
                 jax.ShapeDtypeStruct((rows, d), BF16), jax.ShapeDtypeStruct((rows, d), BF16),
                 jax.ShapeDtypeStruct((8, d), F32)]
    out_specs = ([pl.BlockSpec((TM, HW), full)] * 4 + [pl.BlockSpec((TM, d), full)] * 2
                 + [pl.BlockSpec((TM, d), lat)] * 3 + [pl.BlockSpec((8, d), fixed)])
    return pl.pallas_call(
        body, name=name, grid=(n_tiles,), out_shape=out_shape, in_specs=in_specs, out_specs=out_specs,
        compiler_params=_cparams(dimension_semantics=("arbitrary",)),
    )(x_lat, *o_list, p, p, p, p, p, p, dz2, y1, modx, norms, onorms, w_br_hg, w_br_gla, w_out)


def _in_projection_bwd(ctx0, x0, dz2, modc, modx, pre1, w_t, pieces, n_ctx_tiles, name):
    d = x0.shape[1]
    rows = ctx0.shape[0] + x0.shape[0]
    lat_rows = dz2.shape[0]
    width = P_WIDTH
    n_pieces = len(pieces)

    def body(*refs):
        ctx_ref, x_ref, dz2_ref, modc_ref, modx_ref, pre_ref, w_ref = refs[:7]
        (dhq_f, dhq_b, dhi_f, dhi_b, dhf_f, dhf_b, dhgate, dgq_f, dgq_b, dgk_f, dgk_b, dgv_f, dgv_b, dggate,
         dghg, dggla, dlr_f, dlr_b) = refs[7:7 + n_pieces]
        dp_ref, gx_ref, stat_ref = refs[7 + n_pieces:]
        i = pl.program_id(0)
        is_ctx = i < n_ctx_tiles
        z = jnp.where(is_ctx, ctx_ref[...], x_ref[...])
        sections = [
            (0, dhq_f[...] + dhq_b[...]), (HW, dhi_f[...] + dhi_b[...]), (2 * HW, dhf_f[...]), (3 * HW, dhf_b[...]),
            (4 * HW, dhgate[...]), (5 * HW, dgq_f[...] + dgq_b[...]), (6 * HW, dgk_f[...] + dgk_b[...]),
            (7 * HW, dgv_f[...] + dgv_b[...]), (8 * HW, dggate[...]),
            (9 * HW, dghg[:, 0:HW]), (10 * HW, dghg[:, HW:2 * HW]),
            (11 * HW, dggla[:, 0:HW]), (12 * HW, dggla[:, HW:2 * HW]), (OFF_LR, dlr_f[...] + dlr_b[...])]
        dh = jnp.zeros((TM, d), F32)
        for off, val in sections:
            w = val.shape[1]
            vb = val.astype(BF16)
            dp_ref[:, off:off + w] = vb
            dh = dh + jnp.dot(vb, w_ref[_w_in_row(off):_w_in_row(off) + w, :], preferred_element_type=F32)
        n, r = _rms(z)
        pre = pre_ref[...]
        scale = jnp.where(is_ctx, modc_ref[1:2, :], modx_ref[1:2, :])
        nw = n * pre
        dshift = _colsum(dh)
        dscale = _colsum(dh * nw)
        dnw = dh * (1.0 + scale)
        dpre = _colsum(dnw * n)
        gx_ref[...] = dz2_ref[...] + _rms_bwd(dnw * pre, n, r)
        zero = jnp.zeros((1, d), F32)

        @pl.when(i == 0)
        def _():
            stat_ref[...] = jnp.zeros_like(stat_ref)

        stat_ref[0:1, :] += jnp.where(is_ctx, zero, dshift)
        stat_ref[1:2, :] += jnp.where(is_ctx, zero, dscale)
        stat_ref[2:3, :] += jnp.where(is_ctx, dshift, zero)
        stat_ref[3:4, :] += jnp.where(is_ctx, dscale, zero)
        stat_ref[4:5, :] += dpre

    full = lambda i: (i, 0)
    lat = lambda i: (jnp.maximum(i - n_ctx_tiles, 0), 0)
    fixed = lambda i: (0, 0)
    piece_specs = [pl.BlockSpec((TM, a.shape[1]), full) for a in pieces]
    in_specs = [pl.BlockSpec((TM, d), lambda i: (jnp.minimum(i, n_ctx_tiles - 1), 0)), pl.BlockSpec((TM, d), lat),
                pl.BlockSpec((TM, d), lat), pl.BlockSpec((8, d), fixed),
                pl.BlockSpec((8, d), fixed), pl.BlockSpec((1, d), fixed), VMEM_SPEC] + piece_specs
    return pl.pallas_call(
        body, name=name, grid=(rows // TM,),
        out_shape=[jax.ShapeDtypeStruct((rows, width), BF16), jax.ShapeDtypeStruct((lat_rows, d), F32),
                   jax.ShapeDtypeStruct((8, d), F32)],
        in_specs=in_specs,
        out_specs=[pl.BlockSpec((TM, width), full), pl.BlockSpec((TM, d), lat), pl.BlockSpec((8, d), fixed)],
        compiler_params=_cparams(dimension_semantics=("arbitrary",)),
    )(ctx0, x0, dz2, modc, modx, pre1, w_t, *pieces)


def _transposed_lhs_matmul(x_ref, dy_ref, o_ref, xt_ref):
    @pl.when(pl.program_id(1) == 0)
    def _():
        xt_ref[...] = x_ref[...].T

    o_ref[...] = jnp.dot(xt_ref[...], dy_ref[...], preferred_element_type=F32)


def _w_in_grad(dp, h1, n_cols, name):
    rows, d = h1.shape
    n_main = OFF_LR // HW
    lr0 = _w_in_row(OFF_LR)

    def body(x_ref, xlr_ref, h_ref, o_hbm, xt_ref, acc_ref, sem):
        i = pl.program_id(0)

        def main_copy(step):
            row = jnp.where(step < 9, step * HW, step * HW + 2 * RANK)
            return pltpu.make_async_copy(acc_ref, o_hbm.at[pl.ds(pl.multiple_of(row, 8), HW), :], sem)

        lr_copy = pltpu.make_async_copy(acc_ref.at[0:2 * RANK, :], o_hbm.at[lr0:lr0 + 2 * RANK, :], sem)

        @pl.when(i < n_main)
        def _():
            xt_ref[...] = x_ref[...].T

        @pl.when(i > 0)
        def _():
            main_copy(i - 1).wait()

        @pl.when(i < n_main)
        def _():
            acc_ref[...] = jnp.dot(xt_ref[...], h_ref[...], preferred_element_type=F32)
            main_copy(i).start()

        @pl.when(i == n_main)
        def _():
            xt_ref[0:128, :] = xlr_ref[...].T
            acc_ref[0:128, :] = jnp.dot(xt_ref[0:128, :], h_ref[...], preferred_element_type=F32)
            lr_copy.start()
            lr_copy.wait()

    return pl.pallas_call(
        body, name=name, grid=(n_main + 1,),
        out_shape=jax.ShapeDtypeStruct((n_cols, d), F32),
        in_specs=[pl.BlockSpec((rows, HW), lambda i: (0, jnp.minimum(i, n_main - 1))),
                  pl.BlockSpec((rows, 128), lambda i: (0, OFF_LR // 128)),
                  pl.BlockSpec((rows, d), lambda i: (0, 0))],
        out_specs=ANY_SPEC,
        scratch_shapes=[pltpu.VMEM((HW, rows), BF16), pltpu.VMEM((HW, d), F32), pltpu.SemaphoreType.DMA],
        compiler_params=_cparams(dimension_semantics=("arbitrary",)),
    )(dp, dp, h1)


def _weight_grad(xs, dy, name, tk=None, tn=512, k_first=0, k_tiles=None):
    rows = dy.shape[0]
    n = dy.shape[1]
    tn_ = min(tn, n)
    tk_ = xs.shape[1] if tk is None else tk
    k_tiles = xs.shape[1] // tk_ if k_tiles is None else k_tiles
    k = k_tiles * tk_

    return pl.pallas_call(
        functools.partial(_transposed_lhs_matmul), name=name, grid=(k_tiles, n // tn_),
        out_shape=jax.ShapeDtypeStruct((k, n), F32),
        in_specs=[pl.BlockSpec((rows, tk_), lambda i, j: (0, i + k_first)),
                  pl.BlockSpec((rows, tn_), lambda i, j: (0, j))],
        out_specs=pl.BlockSpec((tk_, tn_), lambda i, j: (i, j)),
        scratch_shapes=[pltpu.VMEM((tk_, rows), BF16)],
        compiler_params=_cparams(dimension_semantics=("parallel", "arbitrary")),
    )(xs, dy)


def _running_sum(x, fw):
    c = x.shape[0]
    row = lax.broadcasted_iota(jnp.int32, (c, 1), 0)
    s = 1
    while s < c:
        if fw:
            x = x + jnp.where(row >= s, pltpu.roll(x, s, axis=0), 0.0)
        else:
            x = x + jnp.where(row < c - s, pltpu.roll(x, c - s, axis=0), 0.0)
        s *= 2
    return x


def _chunk_terms(q, k, g, fw):
    c = CHUNK
    r = lax.broadcasted_iota(jnp.int32, (c, c), 0)
    s = lax.broadcasted_iota(jnp.int32, (c, c), 1)
    causal = (s <= r) if fw else (s >= r)
    causal_t = (s >= r) if fw else (s <= r)
    cum = _running_sum(g, fw)
    row = lax.broadcasted_iota(jnp.int32, (c, 1), 0)
    pos = row if fw else (c - 1 - row)
    starts = [None]
    for j in range(1, NSUB):
        rj = SUB * j - 1 if fw else c - SUB * j
        starts.append(cum[rj:rj + 1, :])
    in_blk = [(pos >= SUB * j) & (pos < SUB * (j + 1)) for j in range(NSUB)]
    e = [jnp.exp(cum)]
    for j in range(1, NSUB):
        e.append(jnp.exp(jnp.where(pos >= SUB * j, cum - starts[j], -1e30)))
    own = jnp.zeros_like(cum)
    for j in range(1, NSUB):
        own = own + jnp.where(in_blk[j], starts[j], 0.0)
    kscale = jnp.exp(own - cum)
    rend = c - 1 if fw else 0
    cend = cum[rend:rend + 1, :]
    tail = jnp.exp(cend - cum)
    qcat = jnp.concatenate([q * e[j] for j in range(NSUB)], axis=1).astype(BF16)
    kt = k * kscale
    km = jnp.concatenate([jnp.where(in_blk[j], kt, 0.0) for j in range(NSUB)], axis=1).astype(BF16)
    return dict(causal=causal, causal_t=causal_t, e=e, in_blk=in_blk, kscale=kscale, cend=cend, tail=tail,
                qcat=qcat, km=km)


def _chunk_fwd(q, k, v, g, st0, fw):
    t = _chunk_terms(q, k, g, fw)
    a = jnp.where(t["causal"], _dot_nt(t["qcat"], t["km"]), 0.0)
    o = _dot(a, v) + _dot_nt(t["qcat"][:, 0:HD], st0)
    st1 = st0 * jnp.exp(t["cend"]) + _dot_tn(v, k * t["tail"])
    return o, st1


def _chunk_bwd(q, k, v, g, st0, do, dst1, fw):
    t = _chunk_terms(q, k, g, fw)
    qcat, km, e = t["qcat"], t["km"], t["e"]
    a_t = jnp.where(t["causal_t"], _dot_nt(km, qcat), 0.0)
    ktail = k * t["tail"]
    dv = _dot(a_t, do) + _dot_nt(ktail, dst1)
    da = jnp.where(t["causal"], _dot_nt(do, v), 0.0)
    da_t = jnp.where(t["causal_t"], _dot_nt(v, do), 0.0)
    dqcat = _dot(da, km)
    dq_inter = e[0] * _dot(do, st0)
    dq = dq_inter
    for j in range(NSUB):
        dq = dq + e[j] * dqcat[:, j * HD:(j + 1) * HD]
    dkm = _dot(da_t, qcat)
    dkt = jnp.zeros_like(k)
    for j in range(NSUB):
        dkt = dkt + jnp.where(t["in_blk"][j], dkm[:, j * HD:(j + 1) * HD], 0.0)
    dk_inter = _dot(v, dst1) * t["tail"]
    dk = dkt * t["kscale"] + dk_inter
    dcum = q * dq_inter - k * dk_inter
    for j in range(NSUB):
        sl = slice(j * HD, (j + 1) * HD)
        dcum = dcum + qcat[:, sl].astype(F32) * dqcat[:, sl] - km[:, sl].astype(F32) * dkm[:, sl]
    ecend = jnp.exp(t["cend"])
    end = ecend * _colsum(st0 * dst1) + _colsum(k * dk_inter)
    dg = _running_sum(dcum, not fw) + end
    dst0 = dst1 * ecend + _dot_tn(do, q * e[0])
    return dq, dk, dv, dg, dst0


def _chunk_index(step, n_ctx_chunks, n_chunks, fw):
    if fw:
        return step
    return jnp.where(step < n_ctx_chunks, n_ctx_chunks - 1 - step, n_chunks - 1 + n_ctx_chunks - step)


def _hg_inputs(hq, hf, lbv, d_idx, sl):
    lb = _sigmoid(lbv[d_idx:d_idx + 1, sl] - lbv[2 + d_idx:3 + d_idx, sl])
    sg = _sigmoid(hf)
    f = lb + (1.0 - lb) * sg
    return _silu(hq), 1.0 - f, jnp.log(f), f, sg, lb


def _scan_fwd(p, side, n_ctx_chunks, fw, branch, name):
    rows = p.shape[0]
    n_chunks = rows // CHUNK
    d_idx = 0 if fw else 1
    hg = branch == "hg"
    cols = (C_HQ, C_HI, C_HF_FW + d_idx) if hg else (C_GQ, C_GK, C_GV)

    def body(*refs):
        if hg:
            a_ref, b_ref, c_ref, lb_ref, o_ref, st_ref, state = refs
        else:
            a_ref, b_ref, c_ref, lr_ref, wgk_ref, bgk_ref, o_ref, st_ref, state = refs
            logits = _dot(lr_ref[...], wgk_ref[...]) + bgk_ref[...]
            g_all = _log_sigmoid(logits) * (1.0 / GATE_NORM)

        @pl.when(pl.program_id(0) == 0)
        def _():
            state[...] = jnp.zeros_like(state)

        for h in range(NH):
            sl = slice(h * HD, (h + 1) * HD)
            if hg:
                q, k, g, _, _, _ = _hg_inputs(a_ref[:, sl], c_ref[:, sl], lb_ref[...], d_idx, sl)
                v = b_ref[:, sl]
            else:
                q, k, v, g = a_ref[:, sl] * (HD ** -0.5), b_ref[:, sl], c_ref[:, sl], g_all[:, sl]
            st0 = state[h]
            st_ref[0, h] = st0
            o, st1 = _chunk_fwd(q, k, v, g, st0, fw)
            o_ref[:, sl] = o
            state[h] = st1

    def cmap(blk):
        return pl.BlockSpec((CHUNK, HW), lambda j: (_chunk_index(j, n_ctx_chunks, n_chunks, fw), blk))

    fixed = lambda j: (0, 0)
    in_specs = [cmap(cols[0]), cmap(cols[1]), cmap(cols[2])]
    if hg:
        in_specs += [pl.BlockSpec((4, HW), fixed)]
        args = (p, p, p, side)
    else:
        in_specs += [pl.BlockSpec((CHUNK, 128), lambda j: (_chunk_index(j, n_ctx_chunks, n_chunks, fw), OFF_LR // 128)),
                     pl.BlockSpec((128, HW), fixed), pl.BlockSpec((1, HW), fixed)]
        args = (p, p, p, p, side[0], side[1])
    return pl.pallas_call(
        body, name=name, grid=(n_chunks,),
        out_shape=[jax.ShapeDtypeStruct((rows, HW), F32), jax.ShapeDtypeStruct((n_chunks, NH, HD, HD), F32)],
        in_specs=in_specs,
        out_specs=[pl.BlockSpec((CHUNK, HW), lambda j: (_chunk_index(j, n_ctx_chunks, n_chunks, fw), 0)),
                   pl.BlockSpec((1, NH, HD, HD), lambda j: (_chunk_index(j, n_ctx_chunks, n_chunks, fw), 0, 0, 0))],
        scratch_shapes=[pltpu.VMEM((NH, HD, HD), F32)],
        compiler_params=_cparams(dimension_semantics=("arbitrary",)),
    )(*args)


def _scan_bwd(p, side, states, d_o, n_ctx_chunks, fw, branch, name):
    rows = p.shape[0]
    n_chunks = rows // CHUNK
    d_idx = 0 if fw else 1
    hg = branch == "hg"
    cols = (C_HQ, C_HI, C_HF_FW + d_idx) if hg else (C_GQ, C_GK, C_GV)

    def body(*refs):
        if hg:
            a_ref, b_ref, c_ref, lb_ref, st_ref, do_ref, da_ref, db_ref, dc_ref, dlb_ref, dstate = refs
        else:
            (a_ref, b_ref, c_ref, lr_ref, wgk_ref, bgk_ref, st_ref, do_ref, da_ref, db_ref, dc_ref, dlr_ref,
             dwgk_ref, dbias_ref, dstate) = refs
            lr = lr_ref[...]
            logits = _dot(lr, wgk_ref[...]) + bgk_ref[...]
            g_all = _log_sigmoid(logits) * (1.0 / GATE_NORM)

        @pl.when(pl.program_id(0) == 0)
        def _():
            dstate[...] = jnp.zeros_like(dstate)
            if hg:
                dlb_ref[...] = jnp.zeros_like(dlb_ref)
            else:
                dwgk_ref[...] = jnp.zeros_like(dwgk_ref)
                dbias_ref[...] = jnp.zeros_like(dbias_ref)

        dg_parts = []
        for h in range(NH):
            sl = slice(h * HD, (h + 1) * HD)
            if hg:
                hq, hf = a_ref[:, sl], c_ref[:, sl]
                q, k, g, f, sg, lb = _hg_inputs(hq, hf, lb_ref[...], d_idx, sl)
                v = b_ref[:, sl]
            else:
                q, k, v, g = a_ref[:, sl] * (HD ** -0.5), b_ref[:, sl], c_ref[:, sl], g_all[:, sl]
            dq, dk, dv, dg, dst0 = _chunk_bwd(q, k, v, g, st_ref[0, h], do_ref[:, sl], dstate[h], fw)
            dstate[h] = dst0
            if hg:
                da_ref[:, sl] = dq * _dsilu(hq)
                db_ref[:, sl] = dv
                df = dg / f - dk
                dc_ref[:, sl] = df * (1.0 - lb) * sg * (1.0 - sg)
                dlb_ref[0:1, sl] += _colsum(df * (1.0 - sg))
            else:
                da_ref[:, sl] = dq * (HD ** -0.5)
                db_ref[:, sl] = dk
                dc_ref[:, sl] = dv
                dg_parts.append(dg)
        if not hg:
            dlogits = jnp.concatenate(dg_parts, axis=1) * (1.0 / GATE_NORM) * (1.0 - _sigmoid(logits))
            dlr_ref[...] = _dot_nt(dlogits, wgk_ref[...])
            dwgk_ref[...] += _dot_tn(lr, dlogits)
            dbias_ref[0:1, :] += _colsum(dlogits)

    def chunk_of(j):
        return _chunk_index(n_chunks - 1 - j, n_ctx_chunks, n_chunks, fw)

    def cmap(blk, width=HW):
        return pl.BlockSpec((CHUNK, width), lambda j: (chunk_of(j), blk))

    fixed = lambda j: (0, 0)
    st_spec = pl.BlockSpec((1, NH, HD, HD), lambda j: (chunk_of(j), 0, 0, 0))
    big = jax.ShapeDtypeStruct((rows, HW), F32)
    if hg:
        in_specs = [cmap(cols[0]), cmap(cols[1]), cmap(cols[2]), pl.BlockSpec((4, HW), fixed), st_spec, cmap(0)]
        args = (p, p, p, side, states, d_o)
        out_shape = [big, big, big, jax.ShapeDtypeStruct((8, HW), F32)]
        out_specs = [cmap(0), cmap(0), cmap(0), pl.BlockSpec((8, HW), fixed)]
    else:
        in_specs = [cmap(cols[0]), cmap(cols[1]), cmap(cols[2]), cmap(OFF_LR // 128, 128),
                    pl.BlockSpec((128, HW), fixed), pl.BlockSpec((1, HW), fixed), st_spec, cmap(0)]
        args = (p, p, p, p, side[0], side[1], states, d_o)
        out_shape = [big, big, big, jax.ShapeDtypeStruct((rows, 128), F32), jax.ShapeDtypeStruct((128, HW), F32),
                     jax.ShapeDtypeStruct((8, HW), F32)]
        out_specs = [cmap(0), cmap(0), cmap(0), cmap(0, 128), pl.BlockSpec((128, HW), fixed),
                     pl.BlockSpec((8, HW), fixed)]
    return pl.pallas_call(
        body, name=name, grid=(n_chunks,), out_shape=out_shape, in_specs=in_specs, out_specs=out_specs,
        scratch_shapes=[pltpu.VMEM((NH, HD, HD), F32)],
        compiler_params=_cparams(dimension_semantics=("arbitrary",)),
    )(*args)


SMALL_ROWS = 56
ROWS_MOD_X = (0, 1, 8, 16, 17, 18)
ROWS_MOD_C = (2, 3)
ROW_PRE1, ROW_POST1, ROW_ONORM, ROW_PRE2, ROW_POST2, ROW_LB, ROW_BGK, ROW_WGK = 4, 9, 10, 19, 20, 24, 32, 40


def _reduce_small(gathered, lb_full, name):
    _, _, d = gathered.shape

    def body(g_ref, lb_ref, sum_ref, dmod_ref, dbmod_ref, dlb_ref):
        total = g_ref[0]
        for b in range(1, N_DEV):
            total = total + g_ref[b]
        sum_ref[...] = total
        dmod_ref[...] = jnp.zeros_like(dmod_ref)
        for m in range(N_MOD):
            col = slice(m * d, (m + 1) * d)
            acc = jnp.zeros((1, d), F32)
            for b in range(N_DEV):
                row = g_ref[b, ROWS_MOD_X[m]:ROWS_MOD_X[m] + 1, :]
                dmod_ref[b:b + 1, col] = row
                acc = acc + row
            if m < 2:
                ctx_row = total[ROWS_MOD_C[m]:ROWS_MOD_C[m] + 1, :]
                dmod_ref[8:9, col] = ctx_row
                acc = acc + ctx_row
            dbmod_ref[:, col] = acc
        lbv = lb_ref[...]
        for dd in range(2):
            lb = _sigmoid(lbv[dd:dd + 1, :] - lbv[2 + dd:3 + dd, :])
            gl = total[ROW_LB:ROW_LB + 1, dd * HW:(dd + 1) * HW] * lb * (1.0 - lb)
            dlb_ref[dd:dd + 1, :] = gl
            dlb_ref[2 + dd:3 + dd, :] = -gl

    return pl.pallas_call(
        body, name=name,
        out_shape=[jax.ShapeDtypeStruct((SMALL_ROWS, d), F32), jax.ShapeDtypeStruct((16, N_MOD * d), F32),
                   jax.ShapeDtypeStruct((1, N_MOD * d), F32), jax.ShapeDtypeStruct((4, HW), F32)],
        in_specs=[VMEM_SPEC] * 2, out_specs=[VMEM_SPEC] * 4, compiler_params=_cparams(),
    )(gathered, lb_full)


def _c_ctx_grad(gathered, c_ctx_row, name):
    def body(g_ref, c_ref, o_ref):
        acc = g_ref[0, 0:1, :]
        for chip in range(1, N_CHIP):
            acc = acc + g_ref[2 * chip, 0:1, :]
        o_ref[...] = acc * _dsilu(c_ref[...])

    return pl.pallas_call(
        body, name=name, out_shape=jax.ShapeDtypeStruct(c_ctx_row.shape, F32),
        in_specs=[VMEM_SPEC] * 2, out_specs=VMEM_SPEC, compiler_params=_cparams(),
    )(gathered, c_ctx_row)


def _relayout_w_in(w):
    pad = jnp.zeros((w.shape[0], 128 - 2 * RANK), w.dtype)
    return jnp.concatenate([w[:, :9 * HW], w[:, 9 * HW + 2 * RANK:], w[:, 9 * HW:9 * HW + 2 * RANK], pad], axis=1)


def _relayout_w_in_rows(wt):
    pad = jnp.zeros((128 - 2 * RANK, wt.shape[1]), wt.dtype)
    return jnp.concatenate([wt[:9 * HW], wt[9 * HW + 2 * RANK:], wt[9 * HW:9 * HW + 2 * RANK], pad], axis=0)


def _w_in_grad_rows(g_main, g_lr):
    return jnp.concatenate([g_main[:9 * HW], g_lr[:2 * RANK], g_main[9 * HW:]], axis=0)


def _w_in_grad_blocks(g_main, g_lr, n_blocks):
    lr0 = 9 * HW
    n = (g_main.shape[1] + 2 * RANK) // n_blocks

    def cols(lo, hi):
        out = []
        if lo < lr0:
            out.append(g_main[:, lo:min(hi, lr0)])
        if hi > lr0 and lo < lr0 + 2 * RANK:
            out.append(g_lr[:, max(lo, lr0) - lr0:min(hi, lr0 + 2 * RANK) - lr0])
        if hi > lr0 + 2 * RANK:
            out.append(g_main[:, max(lo, lr0 + 2 * RANK) - 2 * RANK:hi - 2 * RANK])
        return out

    return jnp.stack([jnp.concatenate(cols(j * n, (j + 1) * n), axis=1) for j in range(n_blocks)])


def _blocked(full, n_blocks):
    k, n = full.shape
    return full.reshape(k, n_blocks, n // n_blocks).transpose(1, 0, 2)


def _unblocked(blocks):
    nb, k, n = blocks.shape
    return blocks.transpose(1, 0, 2).reshape(k, nb * n)


def _sample_front(x0, ctx0, modc, modx, norm_pre1, lb_full, gla_side, w_in_r):
    ctx_len = ctx0.shape[0]
    n_ctx_tiles = ctx_len // TM
    n_ctx_chunks = ctx_len // CHUNK
    h1, p = _in_projection(ctx0, x0, modc, modx, norm_pre1, w_in_r, n_ctx_tiles, "in_projection")
    o_hg_fw, st_hg_fw = _scan_fwd(p, lb_full, n_ctx_chunks, True, "hg", "scan_hg_fw")
    o_hg_bw, st_hg_bw = _scan_fwd(p, lb_full, n_ctx_chunks, False, "hg", "scan_hg_bw")
    o_gla_fw, st_gla_fw = _scan_fwd(p, gla_side[0], n_ctx_chunks, True, "gla", "scan_gla_fw")
    o_gla_bw, st_gla_bw = _scan_fwd(p, gla_side[1], n_ctx_chunks, False, "gla", "scan_gla_bw")
    return dict(h1=h1, p=p, o_list=[o_hg_fw, o_hg_bw, o_gla_fw, o_gla_bw],
                states=[st_hg_fw, st_hg_bw, st_gla_fw, st_gla_bw])


def _sample_back(reduce, front, x0, ctx0, target0, modc, modx, norm_pre1, norms, onorms, lb_full, gla_side, w_in_r,
                 wbh, wbg, wout, wg, wu, wd):
    seq, d = x0.shape
    ctx_len = ctx0.shape[0]
    n_ctx_tiles = ctx_len // TM
    n_tiles = (ctx_len + seq) // TM
    n_ctx_chunks = ctx_len // CHUNK
    h1, p, o_list = front["h1"], front["p"], front["o_list"]
    st_hg_fw, st_hg_bw, st_gla_fw, st_gla_bw = front["states"]
    z2, y1, merged, og_hg, og_gla = _mixer_tail_fwd(x0, p, o_list, modx, norms, onorms, wbh, wbg, wout, n_ctx_tiles,
                                                    "mixer_tail")
    loss_part, dz2, h2, a_act, du, dv, dy2, stat_ffn = _ffn_fwd_bwd(z2, modx, norms, wg, wu, wd, target0, "ffn")
    dff = wg.shape[0]
    tok = reduce("ffn", [_weight_grad(du, h2, "grad_w_ff_gate", tk=dff // 2, tn=d),
                         _weight_grad(dv, h2, "grad_w_ff_up", tk=dff // 2, tn=d),
                         _weight_grad(a_act, dy2, "grad_w_ff_down", tk=dff // 2)])

    (d_ohg, d_ogla, d_hgate, d_ggate, d_ghg, d_ggla, dy1, db_hg, db_gla, stat_mix) = _mixer_tail_bwd(
        x0, p, o_list, dz2, y1, modx + tok, norms, onorms, wbh, wbg, wout, n_ctx_tiles, n_tiles, "mixer_tail_bwd")
    tok = reduce("mix", [_weight_grad(og_hg, db_hg, "grad_w_br_hg"), _weight_grad(og_gla, db_gla, "grad_w_br_gla"),
                         _weight_grad(merged, dy1, "grad_w_out")])
    lb_b = lb_full + tok
    gla_b = [(wgk, bias + tok) for wgk, bias in gla_side]
    dhq_f, dhi_f, dhf_f, dlb_f = _scan_bwd(p, lb_b, st_hg_fw, d_ohg, n_ctx_chunks, True, "hg", "scan_hg_fw_bwd")
    dhq_b, dhi_b, dhf_b, dlb_b = _scan_bwd(p, lb_b, st_hg_bw, d_ohg, n_ctx_chunks, False, "hg", "scan_hg_bw_bwd")
    dgq_f, dgk_f, dgv_f, dlr_f, dwgk_f, dbgk_f = _scan_bwd(p, gla_b[0], st_gla_fw, d_ogla, n_ctx_chunks, True, "gla",
                                                           "scan_gla_fw_bwd")
    dgq_b, dgk_b, dgv_b, dlr_b, dwgk_b, dbgk_b = _scan_bwd(p, gla_b[1], st_gla_bw, d_ogla, n_ctx_chunks, False, "gla",
                                                           "scan_gla_bw_bwd")
    pieces = [dhq_f, dhq_b, dhi_f, dhi_b, dhf_f, dhf_b, d_hgate, dgq_f, dgq_b, dgk_f, dgk_b, dgv_f, dgv_b, d_ggate,
              d_ghg, d_ggla, dlr_f, dlr_b]
    dp, grad_x, stat_in = _in_projection_bwd(ctx0, x0, dz2, modc, modx, norm_pre1, w_in_r, pieces, n_ctx_tiles,
                                             "in_projection_bwd")

    reduce("small", dict(stat_in=stat_in, stat_mix=stat_mix, stat_ffn=stat_ffn, dlb=(dlb_f, dlb_b),
                         dwgk=(dwgk_f, dwgk_b), dbgk=(dbgk_f, dbgk_b)))
    reduce("in", [_w_in_grad(dp, h1, w_in_r.shape[0], "grad_w_in")])
    return dict(
        loss_part=loss_part, grad_x=grad_x, stat_in=stat_in, stat_mix=stat_mix, stat_ffn=stat_ffn,
        dlb=(dlb_f, dlb_b), dwgk=(dwgk_f, dwgk_b), dbgk=(dbgk_f, dbgk_b))


def kernel(x, c, ctx, c_ctx, w_mod, b_mod, norm_pre1, norm_post1, norm_pre2, norm_post2, w_in, hg_lb, hg_onorm, gla_w_gk, gla_b_gk, gla_onorm, w_br_hg, w_br_gla, w_out, w_ff_gate, w_ff_up, w_ff_down, loss_target, m_c_ctx, m_w_mod, m_b_mod, m_norm_pre1, m_norm_post1, m_norm_pre2, m_norm_post2, m_w_in, m_hg_lb, m_hg_onorm, m_gla_w_gk, m_gla_b_gk, m_gla_onorm, m_w_br_hg, m_w_br_gla, m_w_out, m_w_ff_gate, m_w_ff_up, m_w_ff_down, v_c_ctx, v_w_mod, v_b_mod, v_norm_pre1, v_norm_post1, v_norm_pre2, v_norm_post2, v_w_in, v_hg_lb, v_hg_onorm, v_gla_w_gk, v_gla_b_gk, v_gla_onorm, v_w_br_hg, v_w_br_gla, v_w_out, v_w_ff_gate, v_w_ff_up, v_w_ff_down):
    seq, d = x.shape[1], x.shape[2]
    ctx_len = ctx.shape[1]
    assert seq % TM == 0 and ctx_len % TM == 0 and d == 2 * HW
    ax, ay, ac = lax.axis_index("x"), lax.axis_index("y"), lax.axis_index("c")
    chip = 2 * ax + ay
    dev = 2 * chip + ac
    c_arr = jnp.reshape(ac, (1,)).astype(jnp.int32)

    nc = d // 128
    pad8 = lambda a: jnp.pad(a, ((0, -a.shape[0] % 8), (0, 0)))
    small1 = jnp.concatenate([c.reshape(nc, 128), pad8(hg_lb.reshape(4, 128)), gla_w_gk.reshape(2 * RANK, 128),
                              pad8(gla_b_gk.reshape(2, 128))], axis=0)
    got1 = _allgather8(small1, "gather_small_params")
    c_all = got1[:, :nc, :].reshape(N_DEV, d)
    per_chip = got1[0::2]
    lb_full = per_chip[:, nc:nc + 4, :].transpose(1, 0, 2).reshape(4, HW)
    wgk_full = per_chip[:, nc + 8:nc + 8 + 2 * RANK, :].transpose(1, 0, 2).reshape(2, RANK, HW)
    bgk_full = per_chip[:, nc + 8 + 2 * RANK:nc + 10 + 2 * RANK, :].transpose(1, 0, 2).reshape(2, HW)
    wgk_pad = [jnp.zeros((128, HW), F32).at[dd * RANK:(dd + 1) * RANK].set(wgk_full[dd]) for dd in range(2)]
    bgk = [bgk_full[dd:dd + 1] for dd in range(2)]

    n_mod_cols = w_mod.shape[2]
    cond = jnp.concatenate([c_all, pad8(c_ctx.reshape(1, d))], axis=0)
    b_cols = lax.dynamic_slice(b_mod, (0, chip * n_mod_cols), (1, n_mod_cols))
    mod_part = _mod_forward(cond, w_mod[0], b_cols, "mod_forward")
    mod_got = _allgather8(mod_part, "gather_mod")
    mod_all = mod_got[0::2].transpose(1, 0, 2).reshape(16, N_CHIP * n_mod_cols)
    modx = pad8(lax.dynamic_slice(mod_all, (dev, 0), (1, N_MOD * d)).reshape(N_MOD, d))
    modc = pad8(mod_all[8].reshape(N_MOD, d))

    chip_arr = jnp.reshape(chip, (1,)).astype(jnp.int32)
    transposed = ("w_in", "w_ff_gate", "w_ff_up")
    view = lambda a, nm: a[0].T if nm in transposed else a[0]
    blocks = [_cast_into_blocks(chip_arr, view(w_, nm), "cast_" + nm) for w_, nm in (
        (w_in, "w_in"), (w_br_hg, "w_br_hg"), (w_br_gla, "w_br_gla"), (w_out, "w_out"), (w_ff_gate, "w_ff_gate"),
        (w_ff_up, "w_ff_up"), (w_ff_down, "w_ff_down"))]
    gathered_in = _gather_blocks(blocks[:1], "gather_w_in", after=[mod_got])
    sems, lands, token = _blocks_start(blocks[1:], "gather_rest_start", after=[gathered_in[0]])
    w_in_r = gathered_in[0].reshape(-1, d)

    norms = jnp.concatenate([norm_pre1, norm_post1, norm_pre2, norm_post2, jnp.zeros((4, d), F32)], axis=0)
    onorms = jnp.zeros((8, d), F32).at[0, :HD].set(hg_onorm[0]).at[1, :HD].set(gla_onorm[0])
    gla_side = [(wgk_pad[dd], bgk[dd]) for dd in range(2)]
    modx = modx + token[0, 0]
    front = _sample_front(x[0], ctx[0], modc, modx, norm_pre1, lb_full, gla_side, w_in_r)
    lands = _blocks_wait(sems, lands, front["o_list"][3], "gather_rest_wait")
    gathered = _blocks_finish(lands, "gather_rest_finish")
    wbh, wbg = _unblocked(gathered[0]), _unblocked(gathered[1])
    wout = gathered[2].reshape(d, d)
    wg, wu, wd = (gathered[i].reshape(-1, d) for i in (3, 4, 5))
    dff = wg.shape[0]
    groups = {"ffn": ["w_ff_gate", "w_ff_up", "w_ff_down"], "mix": ["w_br_hg", "w_br_gla", "w_out"], "in": ["w_in"]}
    row_sharded = {"w_out": d // N_CHIP, "w_ff_down": dff // N_CHIP, "w_ff_gate": dff // N_CHIP,
                   "w_ff_up": dff // N_CHIP, "w_in": w_in.shape[2]}
    in_flight = {}

    small = {}

    def reduce_small(stats):
        small2 = jnp.concatenate([
            stats["stat_in"], stats["stat_mix"], stats["stat_ffn"],
            jnp.concatenate(stats["dlb"], axis=1), jnp.concatenate(stats["dbgk"], axis=1),
            jnp.concatenate([stats["dwgk"][0][0:RANK], stats["dwgk"][1][RANK:2 * RANK]], axis=1)], axis=0)
        assert small2.shape[0] == SMALL_ROWS
        got2 = _allgather8(small2, "gather_small_grads")
        total, dmod_all, g_b_mod, g_lb_full = _reduce_small(got2, lb_full, "reduce_small")
        dmod_cols = lax.dynamic_slice(dmod_all, (0, chip * n_mod_cols), (16, n_mod_cols))
        g_w_mod, cctx_part = _mod_backward(cond, w_mod[0], dmod_cols, "mod_backward")
        got3 = _allgather8(cctx_part, "gather_c_ctx_grad")
        g_c_ctx = _c_ctx_grad(got3, c_ctx.reshape(1, d), "c_ctx_grad")
        small.update(total=total, g_b_mod=g_b_mod, g_lb_full=g_lb_full, g_w_mod=g_w_mod, g_c_ctx=g_c_ctx)

    def reduce(group, grads):
        if group == "small":
            return reduce_small(grads)
        nms = groups[group]
        full = [g.reshape(N_CHIP, row_sharded[nm], d) if nm in row_sharded else _blocked(g, N_CHIP)
                for g, nm in zip(grads, nms)]
        from_sibling = _send_other_half(full, "grads_to_sibling_" + group)
        pairs = [_pair_sum(c_arr, f, r_, "pair_sum_" + nm) for f, r_, nm in zip(full, from_sibling, nms)]
        after = [small["g_c_ctx"], small["total"]] if group == "in" else []
        sems_, pairs, lands_, token_ = _scatter_start(pairs, "grads_to_owner_start_" + group, after)
        in_flight[group] = (sems_, pairs, lands_, token_)
        return token_[0, 0]

    r = _sample_back(reduce, front, x[0], ctx[0], loss_target[0], modc, modx, norm_pre1, norms, onorms, lb_full,
                     gla_side, w_in_r, wbh, wbg, wout, wg, wu, wd)
    loss_part, grad_x, stat_in, stat_mix, stat_ffn = (r[k] for k in ("loss_part", "grad_x", "stat_in", "stat_mix",
                                                                     "stat_ffn"))
    (dlb_f, dlb_b), (dwgk_f, dwgk_b), (dbgk_f, dbgk_b) = r["dlb"], r["dwgk"], r["dbgk"]

    weights = dict(w_in=(w_in, m_w_in, v_w_in), w_br_hg=(w_br_hg, m_w_br_hg, v_w_br_hg),
                   w_br_gla=(w_br_gla, m_w_br_gla, v_w_br_gla), w_out=(w_out, m_w_out, v_w_out),
                   w_ff_gate=(w_ff_gate, m_w_ff_gate, v_w_ff_gate), w_ff_up=(w_ff_up, m_w_ff_up, v_w_ff_up),
                   w_ff_down=(w_ff_down, m_w_ff_down, v_w_ff_down))
    names = ["w_in", "w_br_hg", "w_br_gla", "w_out", "w_ff_gate", "w_ff_up", "w_ff_down"]
    big = {}

    def finish(group, after):
        sems_, pairs, lands_, _ = in_flight[group]
        pairs, lands_ = _scatter_wait(sems_, pairs, lands_, after, "grads_to_owner_wait_" + group)
        own_half = [_sum_owner(chip_arr, pr, g, "chip_sum_" + nm) for pr, g, nm in zip(pairs, lands_, groups[group])]
        other_half = _swap_with_sibling(own_half, "halves_to_sibling_" + group)
        for nm, own, oth in zip(groups[group], own_half, other_half):
            w_, m_, v_ = (view(a, nm) for a in weights[nm])
            res = _adamw_halves(c_arr, own, oth, w_, m_, v_, "adamw_" + nm)
            big[nm] = [r_.T[None] if nm in transposed else r_[None] for r_ in res]
        return big[groups[group][-1]][1]

    token_in = in_flight["in"][3]
    done_ffn = finish("ffn", [token_in])
    done_mix = finish("mix", [done_ffn])

    total, g_b_mod, g_lb_full, g_w_mod, g_c_ctx = (small[k] for k in ("total", "g_b_mod", "g_lb_full", "g_w_mod",
                                                                      "g_c_ctx"))
    g_pre1, g_post1, g_pre2, g_post2 = (total[r_:r_ + 1] for r_ in (ROW_PRE1, ROW_POST1, ROW_PRE2, ROW_POST2))
    g_hg_on, g_gla_on = total[ROW_ONORM:ROW_ONORM + 1, 0:HD], total[ROW_ONORM:ROW_ONORM + 1, HD:2 * HD]
    n_lb = hg_lb.shape[2]
    g_hg_lb = lax.dynamic_slice(g_lb_full, (0, chip * n_lb), (4, n_lb))
    g_bgk = lax.dynamic_slice(total[ROW_BGK:ROW_BGK + 1].reshape(2, HW), (0, chip * n_lb), (2, n_lb))
    g_wgk_full = total[ROW_WGK:ROW_WGK + RANK].reshape(RANK, 2, HW).transpose(1, 0, 2).reshape(2 * RANK, HW)
    g_wgk = lax.dynamic_slice(g_wgk_full, (0, chip * n_lb), (2 * RANK, n_lb))

    small_items = [
        (g_c_ctx, c_ctx.reshape(1, d), m_c_ctx.reshape(1, d), v_c_ctx.reshape(1, d)),
        (g_b_mod, b_mod, m_b_mod, v_b_mod),
        (g_pre1, norm_pre1, m_norm_pre1, v_norm_pre1),
        (g_post1, norm_post1, m_norm_post1, v_norm_post1),
        (g_pre2, norm_pre2, m_norm_pre2, v_norm_pre2),
        (g_post2, norm_post2, m_norm_post2, v_norm_post2),
        (g_hg_lb, hg_lb.reshape(4, n_lb), m_hg_lb.reshape(4, n_lb), v_hg_lb.reshape(4, n_lb)),
        (g_hg_on, hg_onorm, m_hg_onorm, v_hg_onorm),
        (g_wgk, gla_w_gk.reshape(2 * RANK, n_lb), m_gla_w_gk.reshape(2 * RANK, n_lb), v_gla_w_gk.reshape(2 * RANK, n_lb)),
        (g_bgk, gla_b_gk.reshape(2, n_lb), m_gla_b_gk.reshape(2, n_lb), v_gla_b_gk.reshape(2, n_lb)),
        (g_gla_on, gla_onorm, m_gla_onorm, v_gla_onorm),
    ]
    small_res = _adamw_whole(small_items, "adamw_small")
    mod_res = _adamw_tiled(g_w_mod, w_mod[0], m_w_mod[0], v_w_mod[0], "adamw_w_mod")
    finish("in", [done_mix, mod_res[0], small_res[0][0]])

    loss = lax.psum(loss_part[0, 0], ("x", "y", "c"))

    shapes = dict(c_ctx=c_ctx.shape, b_mod=b_mod.shape, norm_pre1=norm_pre1.shape, norm_post1=norm_post1.shape,
                  norm_pre2=norm_pre2.shape, norm_post2=norm_post2.shape, hg_lb=hg_lb.shape, hg_onorm=hg_onorm.shape,
                  gla_w_gk=gla_w_gk.shape, gla_b_gk=gla_b_gk.shape, gla_onorm=gla_onorm.shape)
    small_names = ["c_ctx", "b_mod", "norm_pre1", "norm_post1", "norm_pre2", "norm_post2", "hg_lb", "hg_onorm",
                   "gla_w_gk", "gla_b_gk", "gla_onorm"]
    grads, deltas, new_m, new_v = {}, {}, {}, {}
    for nm, item, res in zip(small_names, small_items, small_res):
        grads[nm] = item[0].reshape(shapes[nm])
        deltas[nm], new_m[nm], new_v[nm] = (r.reshape(shapes[nm]) for r in res)
    grads["w_mod"] = g_w_mod[None]
    deltas["w_mod"], new_m["w_mod"], new_v["w_mod"] = (r[None] for r in mod_res)
    for nm in names:
        grads[nm], deltas[nm], new_m[nm], new_v[nm] = big[nm]
    order = ["c_ctx", "w_mod", "b_mod", "norm_pre1", "norm_post1", "norm_pre2", "norm_post2", "w_in", "hg_lb",
             "hg_onorm", "gla_w_gk", "gla_b_gk", "gla_onorm", "w_br_hg", "w_br_gla", "w_out", "w_ff_gate", "w_ff_up",
             "w_ff_down"]
    return (loss, grad_x[None], *[grads[n] for n in order], *[deltas[n] for n in order],
            *[new_m[n] for n in order], *[new_v[n] for n in order])


def _weight_grad_cols(xs, dy, n_cols, name, tn=512):
    rows = dy.shape[0]
    k = tk = xs.shape[1]

    return pl.pallas_call(
        functools.partial(_transposed_lhs_matmul), name=name, grid=(k // tk, n_cols // tn),
        out_shape=jax.ShapeDtypeStruct((k, n_cols), F32),
        in_specs=[pl.BlockSpec((rows, tk), lambda i, j: (0, i)), pl.BlockSpec((rows, tn), lambda i, j: (0, j))],
        out_specs=pl.BlockSpec((tk, tn), lambda i, j: (i, j)),
        scratch_shapes=[pltpu.VMEM((tk, rows), BF16)],
        compiler_params=_cparams(dimension_semantics=("parallel", "arbitrary")),
    )(xs, dy)
```

```python
import functools

import jax
import jax.numpy as jnp
from jax import lax
from jax.experimental import pallas as pl
from jax.experimental.pallas import tpu as pltpu

F32 = jnp.float32
BF16 = jnp.bfloat16
HIGHEST = lax.Precision.HIGHEST
MESH = pl.DeviceIdType.MESH

EPS = 1e-6
CHUNK = 64
SUB = 16
NSUB = CHUNK // SUB
NH = 4
HD = 128
HW = NH * HD
RANK = 16
GATE_NORM = 16.0
N_MOD = 6
TM = 256
TM_FFN = 128
N_DEV = 8
N_CHIP = 4
VMEM_LIMIT = 56 * 1024 * 1024

ADAM_LR = 0.001
ADAM_B1 = 0.9
ADAM_B2 = 0.999
ADAM_EPS = 1e-08
ADAM_WD = 0.01
ADAM_STEP = 10

VMEM_SPEC = pl.BlockSpec(memory_space=pltpu.VMEM)
ANY_SPEC = pl.BlockSpec(memory_space=pl.ANY)
HBM_SPEC = pl.BlockSpec(memory_space=pltpu.HBM)
SEM_SPEC = pl.BlockSpec(memory_space=pltpu.SEMAPHORE)
EFFECT = pltpu.SideEffectType.DATAFLOW_SIDE_EFFECTING


def _cparams(**kw):
    return pltpu.CompilerParams(vmem_limit_bytes=VMEM_LIMIT, **kw)


def _dot(a, b):
    return jnp.dot(a.astype(BF16), b.astype(BF16), preferred_element_type=F32)


def _dot_nt(a, b):
    return lax.dot_general(a.astype(BF16), b.astype(BF16), (((1,), (1,)), ((), ())), preferred_element_type=F32)


def _dot_tn(a, b):
    return lax.dot_general(a.astype(BF16), b.astype(BF16), (((0,), (0,)), ((), ())), preferred_element_type=F32)


def _sigmoid(x):
    return 1.0 / (1.0 + jnp.exp(-x))


def _silu(x):
    return x * _sigmoid(x)


def _dsilu(x):
    s = _sigmoid(x)
    return s * (1.0 + x * (1.0 - s))


def _log_sigmoid(x):
    return jnp.minimum(x, 0.0) - jnp.log(1.0 + jnp.exp(-jnp.abs(x)))


def _colsum(a):
    return jnp.sum(a, axis=0, keepdims=True)


def _rms(a):
    r = lax.rsqrt(jnp.mean(a * a, axis=-1, keepdims=True) + EPS)
    return a * r, r


def _rms_bwd(dn, n, r):
    return r * (dn - n * jnp.mean(dn * n, axis=-1, keepdims=True))


def _place():
    x, y, c = lax.axis_index("x"), lax.axis_index("y"), lax.axis_index("c")
    chips = [(1 - x, y), (x, 1 - y), (1 - x, 1 - y)]
    return x, y, c, chips


def _allgather8(v, name):
    rows, cols = v.shape

    def body(x_ref, out_ref, send_sems, recv_sems, local_sem):
        x, y, c, chips = _place()
        me, sibling = (x, y, c), (x, y, 1 - c)

        def blk(px, py, pc):
            return out_ref.at[4 * px + 2 * py + pc]

        def copy(k, block, to, src=None):
            return pltpu.make_async_remote_copy(
                src_ref=blk(*block) if src is None else src, dst_ref=blk(*block),
                send_sem=send_sems.at[k], recv_sem=recv_sems.at[k], device_id=to, device_id_type=MESH)

        mine = pltpu.make_async_copy(x_ref, blk(*me), local_sem)
        mine.start()
        first = [copy(0, me, sibling, src=x_ref)]
        first += [copy(1 + j, me, (*chip, c), src=x_ref) for j, chip in enumerate(chips)]
        for cp in first:
            cp.start()
        passed = [copy(4 + j, (*chip, c), sibling) for j, chip in enumerate(chips)]
        for j, chip in enumerate(chips):
            copy(1 + j, (*chip, c), me).wait_recv()
            passed[j].start()
        copy(0, sibling, me).wait_recv()
        for j, chip in enumerate(chips):
            copy(4 + j, (*chip, 1 - c), me).wait_recv()
        for cp in first + passed:
            cp.wait_send()
        mine.wait()

    return pl.pallas_call(
        body, name=name,
        out_shape=jax.ShapeDtypeStruct((N_DEV, rows, cols), v.dtype),
        in_specs=[VMEM_SPEC], out_specs=VMEM_SPEC,
        scratch_shapes=[pltpu.SemaphoreType.DMA((7,)), pltpu.SemaphoreType.DMA((7,)), pltpu.SemaphoreType.DMA],
    )(v)


def _cast_into_blocks(chip_arr, w, name):
    rows, cols = w.shape
    tr = _row_tile(rows, 16, 256)

    def body(chip_ref, w_ref, o_ref):
        o_ref[0] = w_ref[...].astype(BF16)

    return pl.pallas_call(
        body, name=name,
        grid_spec=pltpu.PrefetchScalarGridSpec(
            num_scalar_prefetch=1, grid=(rows // tr,),
            in_specs=[pl.BlockSpec((tr, cols), lambda i, chip_ref: (i, 0))],
            out_specs=pl.BlockSpec((1, tr, cols), lambda i, chip_ref: (chip_ref[0], i, 0))),
        out_shape=jax.ShapeDtypeStruct((N_CHIP, rows, cols), BF16),
        compiler_params=_cparams(dimension_semantics=("parallel",)),
    )(chip_arr, w)


def _halved_by_rows(shape):
    return (shape[1] // 2) % 16 == 0


def _half_of(ref, pc, block=None):
    lead = slice(None) if block is None else block
    if _halved_by_rows(ref.shape):
        h = ref.shape[1] // 2
        return ref.at[lead, pl.ds(pl.multiple_of(pc * h, 16), h), :]
    h = ref.shape[2] // 2
    return ref.at[lead, :, pl.ds(pl.multiple_of(pc * h, 128), h)]


def _half_shape(shape):
    return (shape[0], shape[1] // 2, shape[2]) if _halved_by_rows(shape) else (shape[0], shape[1], shape[2] // 2)


def _half_rows(ref, chip_id, pc):
    return _half_of(ref, pc, chip_id)


def _gather_blocks(lands, name, after=()):
    n = len(lands)
    n_in = n + len(after)

    def body(*refs):
        outs = refs[n_in:n_in + n]
        send_sems, recv_sems = refs[n_in + n:]
        x, y, c, chips = _place()
        me_chip = 2 * x + y
        sibling = (x, y, 1 - c)

        def copy(k, j, chip_id, pc, to):
            return pltpu.make_async_remote_copy(
                src_ref=_half_rows(outs[k], chip_id, pc), dst_ref=_half_rows(outs[k], chip_id, pc),
                send_sem=send_sems.at[k, j], recv_sem=recv_sems.at[k, j], device_id=to, device_id_type=MESH)

        started = []
        for k in range(n):
            for j, chip in enumerate(chips):
                cp = copy(k, j, me_chip, c, (*chip, c))
                cp.start()
                started.append(cp)
        for k in range(n):
            for j, (px, py) in enumerate(chips):
                copy(k, j, 2 * px + py, c, sibling).wait_recv()
                cp = copy(k, 3 + j, 2 * px + py, c, sibling)
                cp.start()
                started.append(cp)
        for k in range(n):
            for j, (px, py) in enumerate(chips):
                copy(k, 3 + j, 2 * px + py, 1 - c, sibling).wait_recv()
        for cp in started:
            cp.wait_send()

    return pl.pallas_call(
        body, name=name,
        out_shape=[jax.ShapeDtypeStruct(l.shape, l.dtype) for l in lands],
        in_specs=[ANY_SPEC] * n_in, out_specs=[ANY_SPEC] * n,
        input_output_aliases={i: i for i in range(n)},
        scratch_shapes=[pltpu.SemaphoreType.DMA((n, 6)), pltpu.SemaphoreType.DMA((n, 6))],
    )(*lands, *after)


def _hbm(a):
    return pltpu.with_memory_space_constraint(a, pltpu.HBM)


def _blocks_start(lands, name, after=()):
    n = len(lands)
    n_sem = 3 * n
    first = n + len(after)

    def body(*refs):
        lnd = refs[:n]
        send_sems, recv_sems = refs[first:first + n_sem], refs[first + n_sem:first + 2 * n_sem]
        token = refs[-1]
        x, y, c, chips = _place()
        me_chip = 2 * x + y
        for k in range(n):
            for j, chip in enumerate(chips):
                pltpu.make_async_remote_copy(
                    src_ref=_half_rows(lnd[k], me_chip, c), dst_ref=_half_rows(lnd[k], me_chip, c),
                    send_sem=send_sems[3 * k + j], recv_sem=recv_sems[3 * k + j],
                    device_id=(*chip, c), device_id_type=MESH).start()
        token[...] = jnp.zeros_like(token)

    out = pl.pallas_call(
        body, name=name,
        out_shape=(*[pltpu.SemaphoreType.DMA(())] * (2 * n_sem),
                   *[pltpu.HBM(l.shape, l.dtype) for l in lands],
                   jax.ShapeDtypeStruct((8, 128), F32)),
        in_specs=[HBM_SPEC] * n + [ANY_SPEC] * len(after),
        out_specs=(*[SEM_SPEC] * (2 * n_sem), *[HBM_SPEC] * n, VMEM_SPEC),
        input_output_aliases={i: 2 * n_sem + i for i in range(n)},
        compiler_params=pltpu.CompilerParams(has_side_effects=EFFECT),
    )(*[_hbm(l) for l in lands], *after)
    return list(out[:2 * n_sem]), list(out[2 * n_sem:2 * n_sem + n]), out[-1]


def _blocks_wait(sems, lands, after, name):
    n = len(lands)
    n_sem = 3 * n

    def body(*refs):
        lnd = refs[:n]
        s_sems, r_sems = refs[n:n + n_sem], refs[n + n_sem:n + 2 * n_sem]
        x, y, c, chips = _place()
        me_chip = 2 * x + y
        for k in range(n):
            for j, (px, py) in enumerate(chips):
                cp = pltpu.make_async_remote_copy(
                    src_ref=_half_rows(lnd[k], me_chip, c), dst_ref=_half_rows(lnd[k], 2 * px + py, c),
                    send_sem=s_sems[3 * k + j], recv_sem=r_sems[3 * k + j],
                    device_id=(px, py, c), device_id_type=MESH)
                cp.wait_send()
                cp.wait_recv()

    out = pl.pallas_call(
        body, name=name,
        out_shape=tuple(pltpu.HBM(l.shape, l.dtype) for l in lands),
        in_specs=[HBM_SPEC] * n + [SEM_SPEC] * (2 * n_sem) + [ANY_SPEC] * len(after),
        out_specs=[HBM_SPEC] * n,
        input_output_aliases={i: i for i in range(n)},
        compiler_params=pltpu.CompilerParams(has_side_effects=EFFECT),
    )(*lands, *sems, *after)
    return list(out)


def _blocks_finish(lands, name):
    n = len(lands)

    def body(*refs):
        lnd = refs[n:2 * n]
        send_sems, recv_sems = refs[2 * n:]
        x, y, c, chips = _place()
        sibling = (x, y, 1 - c)

        def copy(k, j, chip_id, pc):
            return pltpu.make_async_remote_copy(
                src_ref=_half_rows(lnd[k], chip_id, pc), dst_ref=_half_rows(lnd[k], chip_id, pc),
                send_sem=send_sems.at[k, j], recv_sem=recv_sems.at[k, j], device_id=sibling, device_id_type=MESH)

        started = []
        for k in range(n):
            for j, (px, py) in enumerate(chips):
                cp = copy(k, j, 2 * px + py, c)
                cp.start()
                started.append(cp)
        for k in range(n):
            for j, (px, py) in enumerate(chips):
                copy(k, j, 2 * px + py, 1 - c).wait_recv()
        for cp in started:
            cp.wait_send()

    out = pl.pallas_call(
        body, name=name,
        out_shape=[jax.ShapeDtypeStruct(l.shape, l.dtype) for l in lands],
        in_specs=[ANY_SPEC] * n, out_specs=[ANY_SPEC] * n,
        input_output_aliases={i: i for i in range(n)},
        scratch_shapes=[pltpu.SemaphoreType.DMA((n, 3)), pltpu.SemaphoreType.DMA((n, 3))],
    )(*lands)
    return list(out)


def _gather_start(shards, name):
    n = len(shards)
    n_sem = 3 * n

    def body(*refs):
        ins, lands = refs[:n], refs[n:2 * n]
        send_sems, recv_sems = refs[2 * n:2 * n + n_sem], refs[2 * n + n_sem:2 * n + 2 * n_sem]
        token = refs[-1]
        x, y, c, chips = _place()
        me_chip = 2 * x + y
        for k in range(n):
            h = shards[k].shape[0] // 2
            rows = pl.ds(pl.multiple_of(c * h, 8), h)
            for j, chip in enumerate(chips):
                pltpu.make_async_remote_copy(
                    src_ref=ins[k].at[rows, :], dst_ref=lands[k].at[me_chip, rows, :],
                    send_sem=send_sems[3 * k + j], recv_sem=recv_sems[3 * k + j],
                    device_id=(*chip, c), device_id_type=MESH).start()
        token[...] = jnp.zeros_like(token)

    lands = [_hbm(lax.empty((N_CHIP,) + s.shape, s.dtype)) for s in shards]
    out = pl.pallas_call(
        body, name=name,
        out_shape=(*[pltpu.SemaphoreType.DMA(())] * (2 * n_sem),
                   *[pltpu.HBM(s.shape, s.dtype) for s in shards],
                   *[pltpu.HBM(l.shape, l.dtype) for l in lands],
                   jax.ShapeDtypeStruct((8, 128), F32)),
        in_specs=[HBM_SPEC] * (2 * n),
        out_specs=(*[SEM_SPEC] * (2 * n_sem), *[HBM_SPEC] * (2 * n), VMEM_SPEC),
        input_output_aliases={i: 2 * n_sem + i for i in range(2 * n)},
        compiler_params=pltpu.CompilerParams(has_side_effects=EFFECT),
    )(*[_hbm(s) for s in shards], *lands)
    sems = list(out[:2 * n_sem])
    return sems, list(out[2 * n_sem:2 * n_sem + n]), list(out[2 * n_sem + n:2 * n_sem + 2 * n]), out[-1]


def _gather_wait(sems, shards, lands, after, name):
    n = len(shards)
    n_sem = 3 * n

    def body(*refs):
        ins, lnd = refs[:n], refs[n:2 * n]
        s_sems, r_sems = refs[2 * n:2 * n + n_sem], refs[2 * n + n_sem:2 * n + 2 * n_sem]
        x, y, c, chips = _place()
        for k in range(n):
            h = shards[k].shape[0] // 2
            rows = pl.ds(pl.multiple_of(c * h, 8), h)
            for j, (px, py) in enumerate(chips):
                cp = pltpu.make_async_remote_copy(
                    src_ref=ins[k].at[rows, :], dst_ref=lnd[k].at[2 * px + py, rows, :],
                    send_sem=s_sems[3 * k + j], recv_sem=r_sems[3 * k + j],
                    device_id=(px, py, c), device_id_type=MESH)
                cp.wait_send()
                cp.wait_recv()

    out = pl.pallas_call(
        body, name=name,
        out_shape=(*[pltpu.HBM(s.shape, s.dtype) for s in shards], *[pltpu.HBM(l.shape, l.dtype) for l in lands]),
        in_specs=[HBM_SPEC] * (2 * n) + [SEM_SPEC] * (2 * n_sem) + [ANY_SPEC],
        out_specs=[HBM_SPEC] * (2 * n),
        input_output_aliases={i: i for i in range(2 * n)},
        compiler_params=pltpu.CompilerParams(has_side_effects=EFFECT),
    )(*shards, *lands, *sems, after)
    return list(out[:n]), list(out[n:])


def _gather_finish(shards, lands, name):
    n = len(shards)

    def body(*refs):
        ins, lnd = refs[:n], refs[2 * n:3 * n]
        send_sems, recv_sems, local_sems = refs[3 * n:]
        x, y, c, chips = _place()
        me_chip = 2 * x + y
        sibling = (x, y, 1 - c)

        def half(k, chip_id, pc):
            h = shards[k].shape[0] // 2
            return lnd[k].at[chip_id, pl.ds(pl.multiple_of(pc * h, 8), h), :]

        def copy(k, j, chip_id, pc):
            return pltpu.make_async_remote_copy(
                src_ref=half(k, chip_id, pc), dst_ref=half(k, chip_id, pc),
                send_sem=send_sems.at[k, j], recv_sem=recv_sems.at[k, j], device_id=sibling, device_id_type=MESH)

        locals_, started = [], []
        for k in range(n):
            cp = pltpu.make_async_copy(ins[k], lnd[k].at[me_chip], local_sems.at[k])
            cp.start()
            locals_.append(cp)
            for j, (px, py) in enumerate(chips):
                cp = copy(k, j, 2 * px + py, c)
                cp.start()
                started.append(cp)
        for k in range(n):
            for j, (px, py) in enumerate(chips):
                copy(k, j, 2 * px + py, 1 - c).wait_recv()
        for cp in started:
            cp.wait_send()
        for cp in locals_:
            cp.wait()

    out = pl.pallas_call(
        body, name=name,
        out_shape=[jax.ShapeDtypeStruct(l.shape, l.dtype) for l in lands],
        in_specs=[ANY_SPEC] * (2 * n), out_specs=[ANY_SPEC] * n,
        input_output_aliases={n + i: i for i in range(n)},
        scratch_shapes=[pltpu.SemaphoreType.DMA((n, 3)), pltpu.SemaphoreType.DMA((n, 3)),
                        pltpu.SemaphoreType.DMA((n,))],
    )(*shards, *lands)
    return list(out)


def _send_other_half(arrs, name):
    n = len(arrs)

    def body(*refs):
        ins, outs = refs[:n], refs[n:2 * n]
        send_sems, recv_sems = refs[2 * n:]
        x, y, c, _ = _place()
        cps = []
        for k in range(n):
            cp = pltpu.make_async_remote_copy(
                src_ref=_half_of(ins[k], 1 - c), dst_ref=outs[k],
                send_sem=send_sems.at[k], recv_sem=recv_sems.at[k], device_id=(x, y, 1 - c), device_id_type=MESH)
            cp.start()
            cps.append(cp)
        for cp in cps:
            cp.wait()

    return pl.pallas_call(
        body, name=name,
        out_shape=[jax.ShapeDtypeStruct(_half_shape(a.shape), a.dtype) for a in arrs],
        in_specs=[ANY_SPEC] * n, out_specs=[ANY_SPEC] * n,
        scratch_shapes=[pltpu.SemaphoreType.DMA((n,)), pltpu.SemaphoreType.DMA((n,))],
    )(*arrs)


def _blocks_to_owner(arrs, name):
    n = len(arrs)

    def body(*refs):
        ins, outs = refs[:n], refs[n:2 * n]
        send_sems, recv_sems, local_sems = refs[2 * n:]
        x, y, c, chips = _place()
        me_chip = 2 * x + y
        locals_, started = [], []
        for k in range(n):
            cp = pltpu.make_async_copy(ins[k].at[me_chip], outs[k].at[me_chip], local_sems.at[k])
            cp.start()
            locals_.append(cp)

        def copy(k, j, src_block, dst_slot, to):
            return pltpu.make_async_remote_copy(
                src_ref=ins[k].at[src_block], dst_ref=outs[k].at[dst_slot],
                send_sem=send_sems.at[k, j], recv_sem=recv_sems.at[k, j], device_id=to, device_id_type=MESH)

        for k in range(n):
            for j, (px, py) in enumerate(chips):
                cp = copy(k, j, 2 * px + py, me_chip, (px, py, c))
                cp.start()
                started.append(cp)
        for k in range(n):
            for j, (px, py) in enumerate(chips):
                copy(k, j, me_chip, 2 * px + py, (px, py, c)).wait_recv()
        for cp in started:
            cp.wait_send()
        for cp in locals_:
            cp.wait()

    return pl.pallas_call(
        body, name=name,
        out_shape=[jax.ShapeDtypeStruct(a.shape, a.dtype) for a in arrs],
        in_specs=[ANY_SPEC] * n, out_specs=[ANY_SPEC] * n,
        scratch_shapes=[pltpu.SemaphoreType.DMA((n, 3)), pltpu.SemaphoreType.DMA((n, 3)),
                        pltpu.SemaphoreType.DMA((n,))],
    )(*arrs)


def _scatter_blocks(arrs, name):
    n = len(arrs)

    def body(*refs):
        ins, outs = refs[:n], refs[n:2 * n]
        send_sems, recv_sems = refs[2 * n:]
        x, y, c, chips = _place()
        me_chip = 2 * x + y

        def copy(k, j, src_block, dst_slot, to):
            return pltpu.make_async_remote_copy(
                src_ref=ins[k].at[src_block], dst_ref=outs[k].at[dst_slot],
                send_sem=send_sems.at[k, j], recv_sem=recv_sems.at[k, j], device_id=to, device_id_type=MESH)

        started = []
        for k in range(n):
            for j, (px, py) in enumerate(chips):
                cp = copy(k, j, 2 * px + py, me_chip, (px, py, c))
                cp.start()
                started.append(cp)
        for k in range(n):
            for j, (px, py) in enumerate(chips):
                copy(k, j, me_chip, 2 * px + py, (px, py, c)).wait_recv()
        for cp in started:
            cp.wait_send()

    return pl.pallas_call(
        body, name=name,
        out_shape=[jax.ShapeDtypeStruct(a.shape, a.dtype) for a in arrs],
        in_specs=[ANY_SPEC] * n, out_specs=[ANY_SPEC] * n,
        scratch_shapes=[pltpu.SemaphoreType.DMA((n, 3)), pltpu.SemaphoreType.DMA((n, 3))],
    )(*arrs)


def _scatter_start(arrs, name, after=()):
    n = len(arrs)
    n_sem = 3 * n
    first = 2 * n + len(after)

    def body(*refs):
        ins, lnd = refs[:n], refs[n:2 * n]
        send_sems, recv_sems = refs[first:first + n_sem], refs[first + n_sem:first + 2 * n_sem]
        token = refs[-1]
        x, y, c, chips = _place()
        me_chip = 2 * x + y
        for k in range(n):
            for j, (px, py) in enumerate(chips):
                pltpu.make_async_remote_copy(
                    src_ref=ins[k].at[2 * px + py], dst_ref=lnd[k].at[me_chip],
                    send_sem=send_sems[3 * k + j], recv_sem=recv_sems[3 * k + j],
                    device_id=(px, py, c), device_id_type=MESH).start()
        token[...] = jnp.zeros_like(token)

    lands = [_hbm(lax.empty(a.shape, a.dtype)) for a in arrs]
    out = pl.pallas_call(
        body, name=name,
        out_shape=(*[pltpu.SemaphoreType.DMA(())] * (2 * n_sem),
                   *[pltpu.HBM(a.shape, a.dtype) for a in arrs], *[pltpu.HBM(a.shape, a.dtype) for a in arrs],
                   jax.ShapeDtypeStruct((8, 128), F32)),
        in_specs=[HBM_SPEC] * (2 * n) + [ANY_SPEC] * len(after),
        out_specs=(*[SEM_SPEC] * (2 * n_sem), *[HBM_SPEC] * (2 * n), VMEM_SPEC),
        input_output_aliases={i: 2 * n_sem + i for i in range(2 * n)},
        compiler_params=pltpu.CompilerParams(has_side_effects=EFFECT),
    )(*[_hbm(a) for a in arrs], *lands, *after)
    base = 2 * n_sem
    return list(out[:base]), list(out[base:base + n]), list(out[base + n:base + 2 * n]), out[-1]


def _scatter_wait(sems, arrs, lands, after, name):
    n = len(arrs)
    n_sem = 3 * n

    def body(*refs):
        ins, lnd = refs[:n], refs[n:2 * n]
        s_sems, r_sems = refs[2 * n:2 * n + n_sem], refs[2 * n + n_sem:2 * n + 2 * n_sem]
        x, y, c, chips = _place()
        for k in range(n):
            for j, (px, py) in enumerate(chips):
                cp = pltpu.make_async_remote_copy(
                    src_ref=ins[k].at[2 * px + py], dst_ref=lnd[k].at[2 * px + py],
                    send_sem=s_sems[3 * k + j], recv_sem=r_sems[3 * k + j],
                    device_id=(px, py, c), device_id_type=MESH)
                cp.wait_send()
                cp.wait_recv()

    out = pl.pallas_call(
        body, name=name,
        out_shape=tuple(pltpu.HBM(a.shape, a.dtype) for a in list(arrs) + list(lands)),
        in_specs=[HBM_SPEC] * (2 * n) + [SEM_SPEC] * (2 * n_sem) + [ANY_SPEC] * len(after),
        out_specs=[HBM_SPEC] * (2 * n),
        input_output_aliases={i: i for i in range(2 * n)},
        compiler_params=pltpu.CompilerParams(has_side_effects=EFFECT),
    )(*arrs, *lands, *sems, *after)
    return list(out[:n]), list(out[n:])


def _sum_owner(chip_arr, pairs, got, name):
    nb, h, cols = got.shape
    tr = _row_tile(h, 16, 256)

    def body(chip_ref, own_ref, a_ref, b_ref, c_ref, o_ref):
        o_ref[...] = ((own_ref[0].astype(F32) + a_ref[0].astype(F32)) + b_ref[0].astype(F32)) + c_ref[0].astype(F32)

    def slot(off):
        return pl.BlockSpec((1, tr, cols), lambda i, chip_ref: ((chip_ref[0] + off) % N_CHIP, i, 0))

    return pl.pallas_call(
        body, name=name,
        grid_spec=pltpu.PrefetchScalarGridSpec(
            num_scalar_prefetch=1, grid=(h // tr,),
            in_specs=[slot(0), slot(1), slot(2), slot(3)],
            out_specs=pl.BlockSpec((tr, cols), lambda i, chip_ref: (i, 0))),
        out_shape=jax.ShapeDtypeStruct((h, cols), F32),
        compiler_params=_cparams(dimension_semantics=("parallel",)),
    )(chip_arr, pairs, got, got, got)


def _swap_with_sibling(arrs, name):
    n = len(arrs)

    def body(*refs):
        ins, outs = refs[:n], refs[n:2 * n]
        send_sems, recv_sems = refs[2 * n:]
        x, y, c, _ = _place()
        cps = []
        for k in range(n):
            cp = pltpu.make_async_remote_copy(
                src_ref=ins[k], dst_ref=outs[k], send_sem=send_sems.at[k], recv_sem=recv_sems.at[k],
                device_id=(x, y, 1 - c), device_id_type=MESH)
            cp.start()
            cps.append(cp)
        for cp in cps:
            cp.wait()

    return pl.pallas_call(
        body, name=name,
        out_shape=[jax.ShapeDtypeStruct(a.shape, a.dtype) for a in arrs],
        in_specs=[ANY_SPEC] * n, out_specs=[ANY_SPEC] * n,
        scratch_shapes=[pltpu.SemaphoreType.DMA((n,)), pltpu.SemaphoreType.DMA((n,))],
    )(*arrs)


def _row_tile(h, mult=8, cap=128):
    for t in range(cap - cap % mult, mult - 1, -mult):
        if h % t == 0:
            return t
    if mult > 8:
        return _row_tile(h, 8, cap)
    raise ValueError(h)


def _cast_bf16(a, name):
    rows, cols = a.shape
    tr = _row_tile(rows, 16, 256)

    def body(a_ref, o_ref):
        o_ref[...] = a_ref[...].astype(BF16)

    return pl.pallas_call(
        body, name=name, grid=(rows // tr,),
        out_shape=jax.ShapeDtypeStruct(a.shape, BF16),
        in_specs=[pl.BlockSpec((tr, cols), lambda i: (i, 0))],
        out_specs=pl.BlockSpec((tr, cols), lambda i: (i, 0)),
        compiler_params=_cparams(dimension_semantics=("parallel",)),
    )(a)


def _pair_sum(c_arr, full, recv, name):
    nb, rows, cols = full.shape

    def body(c_ref, f_ref, r_ref, o_ref):
        o_ref[...] = (f_ref[...] + r_ref[...]).astype(BF16)

    if _halved_by_rows(full.shape):
        h = rows // 2
        tr = _row_tile(h, 16, 256)
        steps = h // tr
        own = pl.BlockSpec((1, tr, cols), lambda b, i, c_ref: (b, c_ref[0] * steps + i, 0))
        half = pl.BlockSpec((1, tr, cols), lambda b, i, c_ref: (b, i, 0))
    else:
        steps = 1
        own = pl.BlockSpec((1, rows, cols // 2), lambda b, i, c_ref: (b, 0, c_ref[0]))
        half = pl.BlockSpec((1, rows, cols // 2), lambda b, i, c_ref: (b, 0, 0))
    return pl.pallas_call(
        body, name=name,
        grid_spec=pltpu.PrefetchScalarGridSpec(
            num_scalar_prefetch=1, grid=(nb, steps), in_specs=[own, half], out_specs=half),
        out_shape=jax.ShapeDtypeStruct(_half_shape(full.shape), BF16),
        compiler_params=_cparams(dimension_semantics=("parallel", "parallel")),
    )(c_arr, full, recv)


def _sum_chips(got, name):
    nb, h, cols = got.shape
    tr = _row_tile(h, 16, 256)

    def body(g_ref, o_ref):
        g = g_ref[...].astype(F32)
        o_ref[...] = ((g[0] + g[1]) + g[2]) + g[3]

    return pl.pallas_call(
        body, name=name, grid=(h // tr,),
        out_shape=jax.ShapeDtypeStruct((h, cols), F32),
        in_specs=[pl.BlockSpec((nb, tr, cols), lambda i: (0, i, 0))],
        out_specs=pl.BlockSpec((tr, cols), lambda i: (i, 0)),
        compiler_params=_cparams(dimension_semantics=("parallel",)),
    )(got)


def _adam_math(g, w, m, v):
    m1 = ADAM_B1 * m + (1.0 - ADAM_B1) * g
    v1 = ADAM_B2 * v + (1.0 - ADAM_B2) * (g * g)
    m_hat = m1 / (1.0 - ADAM_B1 ** ADAM_STEP)
    v_hat = v1 / (1.0 - ADAM_B2 ** ADAM_STEP)
    delta = -ADAM_LR * (m_hat / (jnp.sqrt(v_hat) + ADAM_EPS) + ADAM_WD * w)
    return delta, m1, v1


def _adamw_halves(c_arr, own, other, w, m, v, name):
    rows, cols = w.shape

    def body(c_ref, own_ref, oth_ref, w_ref, m_ref, v_ref, g_out, d_out, m_out, v_out):
        g = jnp.where(pl.program_id(0) == c_ref[0], own_ref[...], oth_ref[...])
        d, m1, v1 = _adam_math(g, w_ref[...], m_ref[...], v_ref[...])
        g_out[...] = g
        d_out[...] = d
        m_out[...] = m1
        v_out[...] = v1

    if own.shape[1] == cols:
        h = rows // 2
        tr = _row_tile(h)
        steps = h // tr
        half_spec = pl.BlockSpec((tr, cols), lambda p, i, c_ref: (i, 0))
        full_spec = pl.BlockSpec((tr, cols), lambda p, i, c_ref: (p * steps + i, 0))
    else:
        tr = _row_tile(rows)
        steps = rows // tr
        half_spec = pl.BlockSpec((tr, cols // 2), lambda p, i, c_ref: (i, 0))
        full_spec = pl.BlockSpec((tr, cols // 2), lambda p, i, c_ref: (i, p))
    return pl.pallas_call(
        body, name=name,
        grid_spec=pltpu.PrefetchScalarGridSpec(
            num_scalar_prefetch=1, grid=(2, steps),
            in_specs=[half_spec, half_spec, full_spec, full_spec, full_spec],
            out_specs=[full_spec] * 4),
        out_shape=[jax.ShapeDtypeStruct(w.shape, F32)] * 4,
        compiler_params=_cparams(dimension_semantics=("parallel", "parallel")),
    )(c_arr, own, other, w, m, v)


def _adamw_whole(items, name):
    n = len(items)

    def body(*refs):
        ins, outs = refs[:4 * n], refs[4 * n:]
        for k in range(n):
            g, w, m, v = (r[...] for r in ins[4 * k:4 * k + 4])
            d, m1, v1 = _adam_math(g, w, m, v)
            outs[3 * k][...] = d
            outs[3 * k + 1][...] = m1
            outs[3 * k + 2][...] = v1

    flat = [a for it in items for a in it]
    shapes = [jax.ShapeDtypeStruct(it[1].shape, F32) for it in items for _ in range(3)]
    out = pl.pallas_call(
        body, name=name, out_shape=shapes,
        in_specs=[VMEM_SPEC] * (4 * n), out_specs=[VMEM_SPEC] * (3 * n),
        compiler_params=_cparams(),
    )(*flat)
    return [tuple(out[3 * k:3 * k + 3]) for k in range(n)]


def _adamw_tiled(g, w, m, v, name):
    rows, cols = w.shape
    tr = _row_tile(rows)

    def body(g_ref, w_ref, m_ref, v_ref, d_out, m_out, v_out):
        d, m1, v1 = _adam_math(g_ref[...], w_ref[...], m_ref[...], v_ref[...])
        d_out[...] = d
        m_out[...] = m1
        v_out[...] = v1

    spec = pl.BlockSpec((tr, cols), lambda i: (i, 0))
    return pl.pallas_call(
        body, name=name, grid=(rows // tr,),
        out_shape=[jax.ShapeDtypeStruct(w.shape, F32)] * 3,
        in_specs=[spec] * 4, out_specs=[spec] * 3,
        compiler_params=_cparams(dimension_semantics=("parallel",)),
    )(g, w, m, v)


def _mod_forward(cond, w_mod, b_mod_cols, name):
    def body(c_ref, w_ref, b_ref, o_ref):
        o_ref[...] = _dot(_silu(c_ref[...]), w_ref[...]) + b_ref[...]

    return pl.pallas_call(
        body, name=name, out_shape=jax.ShapeDtypeStruct((cond.shape[0], w_mod.shape[1]), F32),
        in_specs=[VMEM_SPEC] * 3, out_specs=VMEM_SPEC, compiler_params=_cparams(),
    )(cond, w_mod, b_mod_cols)


def _mod_backward(cond, w_mod, dmod_cols, name):
    def body(c_ref, w_ref, d_ref, gw_ref, gc_ref):
        s = _silu(c_ref[...])
        d = d_ref[...]
        gw_ref[...] = _dot_tn(s, d)
        gc_ref[...] = _dot_nt(d[8:16, :], w_ref[...])

    return pl.pallas_call(
        body, name=name,
        out_shape=[jax.ShapeDtypeStruct(w_mod.shape, F32), jax.ShapeDtypeStruct((8, w_mod.shape[0]), F32)],
        in_specs=[VMEM_SPEC] * 3, out_specs=[VMEM_SPEC] * 2, compiler_params=_cparams(),
    )(cond, w_mod, dmod_cols)


def _col_chunks(width, step=512):
    return [(s, min(step, width - s)) for s in range(0, width, step)]


def _w_in_row(p_off):
    if p_off < 9 * HW:
        return p_off
    return 9 * HW if p_off == OFF_LR else p_off + 2 * RANK


def _in_projection(ctx0, x0, modc, modx, pre1, w_t, n_ctx_tiles, name):
    d = x0.shape[1]
    rows = ctx0.shape[0] + x0.shape[0]
    width = P_WIDTH

    def body(ctx_ref, x_ref, modc_ref, modx_ref, pre_ref, w_ref, h_ref, p_ref):
        is_ctx = pl.program_id(0) < n_ctx_tiles
        n, _ = _rms(jnp.where(is_ctx, ctx_ref[...], x_ref[...]))
        shift = jnp.where(is_ctx, modc_ref[0:1, :], modx_ref[0:1, :])
        scale = jnp.where(is_ctx, modc_ref[1:2, :], modx_ref[1:2, :])
        h = (n * pre_ref[...] * (1.0 + scale) + shift).astype(BF16)
        h_ref[...] = h
        for s, w in _col_chunks(width):
            p_ref[:, s:s + w] = _dot_nt(h, w_ref[_w_in_row(s):_w_in_row(s) + w, :])

    row = lambda i: (i, 0)
    fixed = lambda i: (0, 0)
    return pl.pallas_call(
        body, name=name, grid=(rows // TM,),
        out_shape=[jax.ShapeDtypeStruct((rows, d), BF16), jax.ShapeDtypeStruct((rows, width), F32)],
        in_specs=[pl.BlockSpec((TM, d), lambda i: (jnp.minimum(i, n_ctx_tiles - 1), 0)),
                  pl.BlockSpec((TM, d), lambda i: (jnp.maximum(i - n_ctx_tiles, 0), 0)),
                  pl.BlockSpec((8, d), fixed), pl.BlockSpec((8, d), fixed), pl.BlockSpec((1, d), fixed), VMEM_SPEC],
        out_specs=[pl.BlockSpec((TM, d), row), pl.BlockSpec((TM, width), row)],
        compiler_params=_cparams(dimension_semantics=("parallel",)),
    )(ctx0, x0, modc, modx, pre1, w_t)


C_HQ, C_HI, C_HF_FW, C_HF_BW, C_HGATE, C_GQ, C_GK, C_GV, C_GGATE = range(9)
OFF_GATE_HG = 9 * HW
OFF_LR = 13 * HW
P_WIDTH = OFF_LR + 128


def _head_norm_fwd(o, w):
    outs, ns, rs = [], [], []
    for h in range(NH):
        n, r = _rms(o[:, h * HD:(h + 1) * HD])
        ns.append(n)
        rs.append(r)
        outs.append(n * w)
    return jnp.concatenate(outs, axis=1), ns, rs


def _mixer_tail(z, o_hg, o_gla, p_hgate, p_ggate, p_gate_hg, p_gate_gla, hg_on, gla_on, wbh, wbg, wout):
    on_hg, n_hg, r_hg = _head_norm_fwd(o_hg, hg_on)
    on_gla, n_gla, r_gla = _head_norm_fwd(o_gla, gla_on)
    og_hg = (on_hg * _silu(p_hgate)).astype(BF16)
    og_gla = (on_gla * _silu(p_ggate)).astype(BF16)
    b_hg = jnp.dot(og_hg, wbh, preferred_element_type=F32)
    b_gla = jnp.dot(og_gla, wbg, preferred_element_type=F32)
    s_hg = _sigmoid(p_gate_hg)
    s_gla = _sigmoid(p_gate_gla)
    merged = (s_hg * b_hg + s_gla * b_gla).astype(BF16)
    y1 = jnp.dot(merged, wout, preferred_element_type=F32)
    return dict(on_hg=on_hg, n_hg=n_hg, r_hg=r_hg, on_gla=on_gla, n_gla=n_gla, r_gla=r_gla, og_hg=og_hg,
                og_gla=og_gla, b_hg=b_hg, b_gla=b_gla, s_hg=s_hg, s_gla=s_gla, merged=merged, y1=y1)


def _mixer_ffn(x_lat, p, o_list, modx, norms, onorms, w_br_hg, w_br_gla, w_out, w_gate, w_up, w_down, target,
               n_ctx_tiles, name):
    rows, d = x_lat.shape
    dff = w_gate.shape[0]
    inv_d = 1.0 / d

    def body(x_ref, ofw_hg, obw_hg, ofw_gla, obw_gla, p_hgate, p_ggate, p_ghg_a, p_ghg_b, p_ggla_a, p_ggla_b,
             modx_ref, norm_ref, on_ref, wbh_ref, wbg_ref, wout_ref, wg_ref, wu_ref, wd_ref, t_ref,
             loss_ref, dz2_ref, y1_ref, mrg_ref, oghg_ref, oggla_ref, h2_ref, a_ref, du_ref, dv_ref, dy2_ref,
             stat_ref):
        i = pl.program_id(0)
        post1, pre2, post2 = norm_ref[1:2, :], norm_ref[2:3, :], norm_ref[3:4, :]
        gate1, shift2, scale2, gate2 = modx_ref[2:3, :], modx_ref[3:4, :], modx_ref[4:5, :], modx_ref[5:6, :]
        p_gate_hg = jnp.concatenate([p_ghg_a[...], p_ghg_b[...]], axis=1)
        p_gate_gla = jnp.concatenate([p_ggla_a[...], p_ggla_b[...]], axis=1)
        t = _mixer_tail(x_ref[...], ofw_hg[...] + obw_hg[...], ofw_gla[...] + obw_gla[...], p_hgate[...],
                        p_ggate[...], p_gate_hg, p_gate_gla, on_ref[0:1, 0:HD], on_ref[1:2, 0:HD],
                        wbh_ref[...], wbg_ref[...], wout_ref[...])
        y1_ref[...] = t["y1"]
        mrg_ref[...] = t["merged"]
        oghg_ref[...] = t["og_hg"]
        oggla_ref[...] = t["og_gla"]
        n1, _ = _rms(t["y1"])
        z2 = x_ref[...] + n1 * post1 * gate1
        n2, r2 = _rms(z2)
        nw2 = n2 * pre2
        h2 = (nw2 * (1.0 + scale2) + shift2).astype(BF16)
        h2_ref[...] = h2
        u = _dot_nt(h2, wg_ref[...])
        v = _dot_nt(h2, wu_ref[...])
        su = _silu(u)
        a = (su * v).astype(BF16)
        a_ref[...] = a
        y2 = jnp.dot(a, wd_ref[...], preferred_element_type=F32)
        n3, r3 = _rms(y2)
        z3 = z2 + n3 * post2 * gate2
        err = z3 - t_ref[...]
        part = 0.5 * inv_d * jnp.sum(err * err)
        dz3 = err * inv_d
        dgate2 = _colsum(dz3 * n3 * post2)
        tt = dz3 * gate2
        dpost2 = _colsum(tt * n3)
        dy2 = _rms_bwd(tt * post2, n3, r3).astype(BF16)
        dy2_ref[...] = dy2
        da = _dot_nt(dy2, wd_ref[...])
        du = (da * v * _dsilu(u)).astype(BF16)
        dv = (da * su).astype(BF16)
        du_ref[...] = du
        dv_ref[...] = dv
        dh2 = (jnp.dot(du, wg_ref[...], preferred_element_type=F32)
               + jnp.dot(dv, wu_ref[...], preferred_element_type=F32))
        dshift2 = _colsum(dh2)
        dscale2 = _colsum(dh2 * nw2)
        dnw2 = dh2 * (1.0 + scale2)
        dpre2 = _colsum(dnw2 * n2)
        dz2_ref[...] = dz3 + _rms_bwd(dnw2 * pre2, n2, r2)

        @pl.when(i == 0)
        def _():
            stat_ref[...] = jnp.zeros_like(stat_ref)
            loss_ref[...] = jnp.zeros_like(loss_ref)

        for r, val in enumerate((dshift2, dscale2, dgate2, dpre2, dpost2)):
            stat_ref[r:r + 1, :] += val
        loss_ref[...] += part
        stat_ref[5:6, 0:128] += part

    tm = TM_FFN
    ctx_tiles = n_ctx_tiles * (TM // tm)
    lat = lambda i: (i, 0)
    full = lambda i: (i + ctx_tiles, 0)
    fixed = lambda i: (0, 0)

    def pcol(blk):
        return pl.BlockSpec((tm, HW), lambda i: (i + ctx_tiles, blk))

    in_specs = ([pl.BlockSpec((tm, d), lat)] + [pl.BlockSpec((tm, HW), full)] * 4
                + [pcol(C_HGATE), pcol(C_GGATE), pcol(9), pcol(10), pcol(11), pcol(12)]
                + [pl.BlockSpec((8, d), fixed), pl.BlockSpec((8, d), fixed), pl.BlockSpec((8, d), fixed)]
                + [VMEM_SPEC] * 6 + [pl.BlockSpec((tm, d), lat)])
    bf = lambda w: jax.ShapeDtypeStruct((rows, w), BF16)
    out_shape = [jax.ShapeDtypeStruct((8, 128), F32), jax.ShapeDtypeStruct((rows, d), F32),
                 jax.ShapeDtypeStruct((rows, d), F32), bf(d), bf(HW), bf(HW), bf(d), bf(dff), bf(dff), bf(dff), bf(d),
                 jax.ShapeDtypeStruct((8, d), F32)]
    out_specs = [pl.BlockSpec((8, 128), fixed), pl.BlockSpec((tm, d), lat), pl.BlockSpec((tm, d), lat),
                 pl.BlockSpec((tm, d), lat), pl.BlockSpec((tm, HW), lat), pl.BlockSpec((tm, HW), lat),
                 pl.BlockSpec((tm, d), lat), pl.BlockSpec((tm, dff), lat), pl.BlockSpec((tm, dff), lat),
                 pl.BlockSpec((tm, dff), lat), pl.BlockSpec((tm, d), lat), pl.BlockSpec((8, d), fixed)]
    return pl.pallas_call(
        body, name=name, grid=(rows // tm,), out_shape=out_shape, in_specs=in_specs, out_specs=out_specs,
        compiler_params=_cparams(dimension_semantics=("arbitrary",)),
    )(x_lat, *o_list, p, p, p, p, p, p, modx, norms, onorms, w_br_hg, w_br_gla, w_out, w_gate, w_up, w_down, target)


def _mixer_tail_fwd(x_lat, p, o_list, modx, norms, onorms, w_br_hg, w_br_gla, w_out, n_ctx_tiles, name):
    rows, d = x_lat.shape

    def body(x_ref, ofw_hg, obw_hg, ofw_gla, obw_gla, p_hgate, p_ggate, p_ghg_a, p_ghg_b, p_ggla_a, p_ggla_b,
             modx_ref, norm_ref, on_ref, wbh_ref, wbg_ref, wout_ref, z2_ref, y1_ref, mrg_ref, oghg_ref, oggla_ref):
        p_gate_hg = jnp.concatenate([p_ghg_a[...], p_ghg_b[...]], axis=1)
        p_gate_gla = jnp.concatenate([p_ggla_a[...], p_ggla_b[...]], axis=1)
        t = _mixer_tail(x_ref[...], ofw_hg[...] + obw_hg[...], ofw_gla[...] + obw_gla[...], p_hgate[...],
                        p_ggate[...], p_gate_hg, p_gate_gla, on_ref[0:1, 0:HD], on_ref[1:2, 0:HD],
                        wbh_ref[...], wbg_ref[...], wout_ref[...])
        y1_ref[...] = t["y1"]
        mrg_ref[...] = t["merged"]
        oghg_ref[...] = t["og_hg"]
        oggla_ref[...] = t["og_gla"]
        n1, _ = _rms(t["y1"])
        z2_ref[...] = x_ref[...] + n1 * norm_ref[1:2, :] * modx_ref[2:3, :]

    lat = lambda i: (i, 0)
    full = lambda i: (i + n_ctx_tiles, 0)
    fixed = lambda i: (0, 0)

    def pcol(blk):
        return pl.BlockSpec((TM, HW), lambda i: (i + n_ctx_tiles, blk))

    in_specs = ([pl.BlockSpec((TM, d), lat)] + [pl.BlockSpec((TM, HW), full)] * 4
                + [pcol(C_HGATE), pcol(C_GGATE), pcol(9), pcol(10), pcol(11), pcol(12)]
                + [pl.BlockSpec((8, d), fixed)] * 3 + [VMEM_SPEC] * 3)
    bf = lambda w: jax.ShapeDtypeStruct((rows, w), BF16)
    f32 = jax.ShapeDtypeStruct((rows, d), F32)
    return pl.pallas_call(
        body, name=name, grid=(rows // TM,), out_shape=[f32, f32, bf(d), bf(HW), bf(HW)], in_specs=in_specs,
        out_specs=[pl.BlockSpec((TM, d), lat)] * 3 + [pl.BlockSpec((TM, HW), lat)] * 2,
        compiler_params=_cparams(dimension_semantics=("parallel",)),
    )(x_lat, *o_list, p, p, p, p, p, p, modx, norms, onorms, w_br_hg, w_br_gla, w_out)


def _ffn_fwd_bwd(z2, modx, norms, w_gate, w_up, w_down, target, name):
    rows, d = z2.shape
    dff = w_gate.shape[0]
    inv_d = 1.0 / d

    def body(z2_ref, modx_ref, norm_ref, wg_ref, wu_ref, wd_ref, t_ref,
             loss_ref, dz2_ref, h2_ref, a_ref, du_ref, dv_ref, dy2_ref, stat_ref):
        i = pl.program_id(0)
        pre2, post2 = norm_ref[2:3, :], norm_ref[3:4, :]
        shift2, scale2, gate2 = modx_ref[3:4, :], modx_ref[4:5, :], modx_ref[5:6, :]
        z2 = z2_ref[...]
        n2, r2 = _rms(z2)
        nw2 = n2 * pre2
        h2 = (nw2 * (1.0 + scale2) + shift2).astype(BF16)
        h2_ref[...] = h2
        u = _dot_nt(h2, wg_ref[...])
        v = _dot_nt(h2, wu_ref[...])
        su = _silu(u)
        a = (su * v).astype(BF16)
        a_ref[...] = a
        y2 = jnp.dot(a, wd_ref[...], preferred_element_type=F32)
        n3, r3 = _rms(y2)
        err = z2 + n3 * post2 * gate2 - t_ref[...]
        part = 0.5 * inv_d * jnp.sum(err * err)
        dz3 = err * inv_d
        dgate2 = _colsum(dz3 * n3 * post2)
        tt = dz3 * gate2
        dpost2 = _colsum(tt * n3)
        dy2 = _rms_bwd(tt * post2, n3, r3).astype(BF16)
        dy2_ref[...] = dy2
        da = _dot_nt(dy2, wd_ref[...])
        du = (da * v * _dsilu(u)).astype(BF16)
        dv = (da * su).astype(BF16)
        du_ref[...] = du
        dv_ref[...] = dv
        dh2 = (jnp.dot(du, wg_ref[...], preferred_element_type=F32)
               + jnp.dot(dv, wu_ref[...], preferred_element_type=F32))
        dshift2 = _colsum(dh2)
        dscale2 = _colsum(dh2 * nw2)
        dnw2 = dh2 * (1.0 + scale2)
        dpre2 = _colsum(dnw2 * n2)
        dz2_ref[...] = dz3 + _rms_bwd(dnw2 * pre2, n2, r2)

        @pl.when(i == 0)
        def _():
            stat_ref[...] = jnp.zeros_like(stat_ref)
            loss_ref[...] = jnp.zeros_like(loss_ref)

        for r, val in enumerate((dshift2, dscale2, dgate2, dpre2, dpost2)):
            stat_ref[r:r + 1, :] += val
        loss_ref[...] += part
        stat_ref[5:6, 0:128] += part

    lat = lambda i: (i, 0)
    fixed = lambda i: (0, 0)
    bf = lambda w: jax.ShapeDtypeStruct((rows, w), BF16)
    return pl.pallas_call(
        body, name=name, grid=(rows // TM,),
        out_shape=[jax.ShapeDtypeStruct((8, 128), F32), jax.ShapeDtypeStruct((rows, d), F32), bf(d), bf(dff), bf(dff),
                   bf(dff), bf(d), jax.ShapeDtypeStruct((8, d), F32)],
        in_specs=[pl.BlockSpec((TM, d), lat), pl.BlockSpec((8, d), fixed), pl.BlockSpec((8, d), fixed)]
        + [VMEM_SPEC] * 3 + [pl.BlockSpec((TM, d), lat)],
        out_specs=[pl.BlockSpec((8, 128), fixed), pl.BlockSpec((TM, d), lat), pl.BlockSpec((TM, d), lat),
                   pl.BlockSpec((TM, dff), lat), pl.BlockSpec((TM, dff), lat), pl.BlockSpec((TM, dff), lat),
                   pl.BlockSpec((TM, d), lat), pl.BlockSpec((8, d), fixed)],
        compiler_params=_cparams(dimension_semantics=("arbitrary",)),
    )(z2, modx, norms, w_gate, w_up, w_down, target)


def _mixer_tail_bwd(x_lat, p, o_list, dz2, y1, modx, norms, onorms, w_br_hg, w_br_gla, w_out, n_ctx_tiles, n_tiles,
                    name):
    rows, d = x_lat.shape
    total = n_tiles * TM

    def body(x_ref, ofw_hg, obw_hg, ofw_gla, obw_gla, p_hgate, p_ggate, p_ghg_a, p_ghg_b, p_ggla_a, p_ggla_b,
             dz2_ref, y1_ref, modx_ref, norm_ref, on_ref, wbh_ref, wbg_ref, wout_ref,
             dohg_ref, dogla_ref, dhgate_ref, dggate_ref, dghg_ref, dggla_ref, dy1_ref, dbhg_ref, dbgla_ref,
             stat_ref):
        i = pl.program_id(0)

        @pl.when(i == 0)
        def _():
            stat_ref[...] = jnp.zeros_like(stat_ref)

        @pl.when(i < n_ctx_tiles)
        def _():
            for ref in (dohg_ref, dogla_ref, dhgate_ref, dggate_ref, dghg_ref, dggla_ref):
                ref[...] = jnp.zeros_like(ref)

        @pl.when(i >= n_ctx_tiles)
        def _():
            post1, gate1 = norm_ref[1:2, :], modx_ref[2:3, :]
            hg_on, gla_on = on_ref[0:1, 0:HD], on_ref[1:2, 0:HD]
            p_gate_hg = jnp.concatenate([p_ghg_a[...], p_ghg_b[...]], axis=1)
            p_gate_gla = jnp.concatenate([p_ggla_a[...], p_ggla_b[...]], axis=1)
            ph, pg = p_hgate[...], p_ggate[...]
            t = _mixer_tail(x_ref[...], ofw_hg[...] + obw_hg[...], ofw_gla[...] + obw_gla[...], ph, pg,
                            p_gate_hg, p_gate_gla, hg_on, gla_on, wbh_ref[...], wbg_ref[...], wout_ref[...])
            dz2 = dz2_ref[...]
            n1, r1 = _rms(y1_ref[...])
            dgate1 = _colsum(dz2 * n1 * post1)
            tt = dz2 * gate1
            dpost1 = _colsum(tt * n1)
            dy1 = _rms_bwd(tt * post1, n1, r1).astype(BF16)
            dy1_ref[...] = dy1
            dmerged = _dot_nt(dy1, wout_ref[...])
            dghg_ref[...] = dmerged * t["b_hg"] * t["s_hg"] * (1.0 - t["s_hg"])
            dggla_ref[...] = dmerged * t["b_gla"] * t["s_gla"] * (1.0 - t["s_gla"])
            db_hg = (dmerged * t["s_hg"]).astype(BF16)
            db_gla = (dmerged * t["s_gla"]).astype(BF16)
            dbhg_ref[...] = db_hg
            dbgla_ref[...] = db_gla
            don_acc = []
            for (db, wb, pgate, on, ns, rs, gain, gate_ref, do_ref) in (
                    (db_hg, wbh_ref, ph, t["on_hg"], t["n_hg"], t["r_hg"], hg_on, dhgate_ref, dohg_ref),
                    (db_gla, wbg_ref, pg, t["on_gla"], t["n_gla"], t["r_gla"], gla_on, dggate_ref, dogla_ref)):
                dog = _dot_nt(db, wb[...])
                gate_ref[...] = dog * on * _dsilu(pgate)
                don = dog * _silu(pgate)
                acc = jnp.zeros((1, HD), F32)
                for h in range(NH):
                    sl = slice(h * HD, (h + 1) * HD)
                    acc = acc + _colsum(don[:, sl] * ns[h])
                    do_ref[:, sl] = _rms_bwd(don[:, sl] * gain, ns[h], rs[h])
                don_acc.append(acc)
            stat_ref[0:1, :] += dgate1
            stat_ref[1:2, :] += dpost1
            stat_ref[2:3, 0:HD] += don_acc[0]
            stat_ref[2:3, HD:2 * HD] += don_acc[1]

    lat = lambda i: (jnp.maximum(i - n_ctx_tiles, 0), 0)
    full = lambda i: (i, 0)
    fixed = lambda i: (0, 0)

    def pcol(blk):
        return pl.BlockSpec((TM, HW), lambda i: (i, blk))

    in_specs = ([pl.BlockSpec((TM, d), lat)] + [pl.BlockSpec((TM, HW), full)] * 4
                + [pcol(C_HGATE), pcol(C_GGATE), pcol(9), pcol(10), pcol(11), pcol(12)]
                + [pl.BlockSpec((TM, d), lat), pl.BlockSpec((TM, d), lat)]
                + [pl.BlockSpec((8, d), fixed)] * 3 + [VMEM_SPEC] * 3)
    f = lambda w: jax.ShapeDtypeStruct((total, w), F32)
    out_shape = [f(HW), f(HW), f(HW), f(HW), f(d), f(d), jax.ShapeDtypeStruct((rows, d), BF16),
                 jax.ShapeDtypeStruct((rows, d), BF16), jax.ShapeDtypeStruct((rows, d), BF16),
                 jax.ShapeDtypeStruct((8, d), F32)]
    out_specs = ([pl.BlockSpec((TM, HW), full)] * 4 + [pl.BlockSpec((TM, d), full)] * 2
                 + [pl.BlockSpec((TM, d), lat)] * 3 + [pl.BlockSpec((8, d), fixed)])
    return pl.pallas_call(
        body, name=name, grid=(n_tiles,), out_shape=out_shape, in_specs=in_specs, out_specs=out_specs,
        compiler_params=_cparams(dimension_semantics=("arbitrary",)),
    )(x_lat, *o_list, p, p, p, p, p, p, dz2, y1, modx, norms, onorms, w_br_hg, w_br_gla, w_out)


def _in_projection_bwd(ctx0, x0, dz2, modc, modx, pre1, w_t, pieces, n_ctx_tiles, name):
    d = x0.shape[1]
    rows = ctx0.shape[0] + x0.shape[0]
    lat_rows = dz2.shape[0]
    width = P_WIDTH
    n_pieces = len(pieces)

    def body(*refs):
        ctx_ref, x_ref, dz2_ref, modc_ref, modx_ref, pre_ref, w_ref = refs[:7]
        (dhq_f, dhq_b, dhi_f, dhi_b, dhf_f, dhf_b, dhgate, dgq_f, dgq_b, dgk_f, dgk_b, dgv_f, dgv_b, dggate,
         dghg, dggla, dlr_f, dlr_b) = refs[7:7 + n_pieces]
        dp_ref, gx_ref, stat_ref = refs[7 + n_pieces:]
        i = pl.program_id(0)
        is_ctx = i < n_ctx_tiles
        z = jnp.where(is_ctx, ctx_ref[...], x_ref[...])
        sections = [
            (0, dhq_f[...] + dhq_b[...]), (HW, dhi_f[...] + dhi_b[...]), (2 * HW, dhf_f[...]), (3 * HW, dhf_b[...]),
            (4 * HW, dhgate[...]), (5 * HW, dgq_f[...] + dgq_b[...]), (6 * HW, dgk_f[...] + dgk_b[...]),
            (7 * HW, dgv_f[...] + dgv_b[...]), (8 * HW, dggate[...]),
            (9 * HW, dghg[:, 0:HW]), (10 * HW, dghg[:, HW:2 * HW]),
            (11 * HW, dggla[:, 0:HW]), (12 * HW, dggla[:, HW:2 * HW]), (OFF_LR, dlr_f[...] + dlr_b[...])]
        dh = jnp.zeros((TM, d), F32)
        for off, val in sections:
            w = val.shape[1]
            vb = val.astype(BF16)
            dp_ref[:, off:off + w] = vb
            dh = dh + jnp.dot(vb, w_ref[_w_in_row(off):_w_in_row(off) + w, :], preferred_element_type=F32)
        n, r = _rms(z)
        pre = pre_ref[...]
        scale = jnp.where(is_ctx, modc_ref[1:2, :], modx_ref[1:2, :])
        nw = n * pre
        dshift = _colsum(dh)
        dscale = _colsum(dh * nw)
        dnw = dh * (1.0 + scale)
        dpre = _colsum(dnw * n)
        gx_ref[...] = dz2_ref[...] + _rms_bwd(dnw * pre, n, r)
        zero = jnp.zeros((1, d), F32)

        @pl.when(i == 0)
        def _():
            stat_ref[...] = jnp.zeros_like(stat_ref)

        stat_ref[0:1, :] += jnp.where(is_ctx, zero, dshift)
        stat_ref[1:2, :] += jnp.where(is_ctx, zero, dscale)
        stat_ref[2:3, :] += jnp.where(is_ctx, dshift, zero)
        stat_ref[3:4, :] += jnp.where(is_ctx, dscale, zero)
        stat_ref[4:5, :] += dpre

    full = lambda i: (i, 0)
    lat = lambda i: (jnp.maximum(i - n_ctx_tiles, 0), 0)
    fixed = lambda i: (0, 0)
    piece_specs = [pl.BlockSpec((TM, a.shape[1]), full) for a in pieces]
    in_specs = [pl.BlockSpec((TM, d), lambda i: (jnp.minimum(i, n_ctx_tiles - 1), 0)), pl.BlockSpec((TM, d), lat),
                pl.BlockSpec((TM, d), lat), pl.BlockSpec((8, d), fixed),
                pl.BlockSpec((8, d), fixed), pl.BlockSpec((1, d), fixed), VMEM_SPEC] + piece_specs
    return pl.pallas_call(
        body, name=name, grid=(rows // TM,),
        out_shape=[jax.ShapeDtypeStruct((rows, width), BF16), jax.ShapeDtypeStruct((lat_rows, d), F32),
                   jax.ShapeDtypeStruct((8, d), F32)],
        in_specs=in_specs,
        out_specs=[pl.BlockSpec((TM, width), full), pl.BlockSpec((TM, d), lat), pl.BlockSpec((8, d), fixed)],
        compiler_params=_cparams(dimension_semantics=("arbitrary",)),
    )(ctx0, x0, dz2, modc, modx, pre1, w_t, *pieces)


def _transposed_lhs_matmul(x_ref, dy_ref, o_ref, xt_ref):
    @pl.when(pl.program_id(1) == 0)
    def _():
        xt_ref[...] = x_ref[...].T

    o_ref[...] = jnp.dot(xt_ref[...], dy_ref[...], preferred_element_type=F32)


def _w_in_grad(dp, h1, n_cols, name):
    rows, d = h1.shape
    n_main = OFF_LR // HW
    lr0 = _w_in_row(OFF_LR)

    def body(x_ref, xlr_ref, h_ref, o_hbm, xt_ref, acc_ref, sem):
        i = pl.program_id(0)

        def main_copy(step):
            row = jnp.where(step < 9, step * HW, step * HW + 2 * RANK)
            return pltpu.make_async_copy(acc_ref, o_hbm.at[pl.ds(pl.multiple_of(row, 8), HW), :], sem)

        lr_copy = pltpu.make_async_copy(acc_ref.at[0:2 * RANK, :], o_hbm.at[lr0:lr0 + 2 * RANK, :], sem)

        @pl.when(i < n_main)
        def _():
            xt_ref[...] = x_ref[...].T

        @pl.when(i > 0)
        def _():
            main_copy(i - 1).wait()

        @pl.when(i < n_main)
        def _():
            acc_ref[...] = jnp.dot(xt_ref[...], h_ref[...], preferred_element_type=F32)
            main_copy(i).start()

        @pl.when(i == n_main)
        def _():
            xt_ref[0:128, :] = xlr_ref[...].T
            acc_ref[0:128, :] = jnp.dot(xt_ref[0:128, :], h_ref[...], preferred_element_type=F32)
            lr_copy.start()
            lr_copy.wait()

    return pl.pallas_call(
        body, name=name, grid=(n_main + 1,),
        out_shape=jax.ShapeDtypeStruct((n_cols, d), F32),
        in_specs=[pl.BlockSpec((rows, HW), lambda i: (0, jnp.minimum(i, n_main - 1))),
                  pl.BlockSpec((rows, 128), lambda i: (0, OFF_LR // 128)),
                  pl.BlockSpec((rows, d), lambda i: (0, 0))],
        out_specs=ANY_SPEC,
        scratch_shapes=[pltpu.VMEM((HW, rows), BF16), pltpu.VMEM((HW, d), F32), pltpu.SemaphoreType.DMA],
        compiler_params=_cparams(dimension_semantics=("arbitrary",)),
    )(dp, dp, h1)


def _weight_grad(xs, dy, name, tk=None, tn=512, k_first=0, k_tiles=None):
    rows = dy.shape[0]
    n = dy.shape[1]
    tn_ = min(tn, n)
    tk_ = xs.shape[1] if tk is None else tk
    k_tiles = xs.shape[1] // tk_ if k_tiles is None else k_tiles
    k = k_tiles * tk_

    return pl.pallas_call(
        functools.partial(_transposed_lhs_matmul), name=name, grid=(k_tiles, n // tn_),
        out_shape=jax.ShapeDtypeStruct((k, n), F32),
        in_specs=[pl.BlockSpec((rows, tk_), lambda i, j: (0, i + k_first)),
                  pl.BlockSpec((rows, tn_), lambda i, j: (0, j))],
        out_specs=pl.BlockSpec((tk_, tn_), lambda i, j: (i, j)),
        scratch_shapes=[pltpu.VMEM((tk_, rows), BF16)],
        compiler_params=_cparams(dimension_semantics=("parallel", "arbitrary")),
    )(xs, dy)


def _running_sum(x, fw):
    c = x.shape[0]
    row = lax.broadcasted_iota(jnp.int32, (c, 1), 0)
    s = 1
    while s < c:
        if fw:
            x = x + jnp.where(row >= s, pltpu.roll(x, s, axis=0), 0.0)
        else:
            x = x + jnp.where(row < c - s, pltpu.roll(x, c - s, axis=0), 0.0)
        s *= 2
    return x


def _chunk_terms(q, k, g, fw):
    c = CHUNK
    r = lax.broadcasted_iota(jnp.int32, (c, c), 0)
    s = lax.broadcasted_iota(jnp.int32, (c, c), 1)
    causal = (s <= r) if fw else (s >= r)
    causal_t = (s >= r) if fw else (s <= r)
    cum = _running_sum(g, fw)
    row = lax.broadcasted_iota(jnp.int32, (c, 1), 0)
    pos = row if fw else (c - 1 - row)
    starts = [None]
    for j in range(1, NSUB):
        rj = SUB * j - 1 if fw else c - SUB * j
        starts.append(cum[rj:rj + 1, :])
    in_blk = [(pos >= SUB * j) & (pos < SUB * (j + 1)) for j in range(NSUB)]
    e = [jnp.exp(cum)]
    for j in range(1, NSUB):
        e.append(jnp.exp(jnp.where(pos >= SUB * j, cum - starts[j], -1e30)))
    own = jnp.zeros_like(cum)
    for j in range(1, NSUB):
        own = own + jnp.where(in_blk[j], starts[j], 0.0)
    kscale = jnp.exp(own - cum)
    rend = c - 1 if fw else 0
    cend = cum[rend:rend + 1, :]
    tail = jnp.exp(cend - cum)
    qcat = jnp.concatenate([q * e[j] for j in range(NSUB)], axis=1).astype(BF16)
    kt = k * kscale
    km = jnp.concatenate([jnp.where(in_blk[j], kt, 0.0) for j in range(NSUB)], axis=1).astype(BF16)
    return dict(causal=causal, causal_t=causal_t, e=e, in_blk=in_blk, kscale=kscale, cend=cend, tail=tail,
                qcat=qcat, km=km)


def _chunk_fwd(q, k, v, g, st0, fw):
    t = _chunk_terms(q, k, g, fw)
    a = jnp.where(t["causal"], _dot_nt(t["qcat"], t["km"]), 0.0)
    o = _dot(a, v) + _dot_nt(t["qcat"][:, 0:HD], st0)
    st1 = st0 * jnp.exp(t["cend"]) + _dot_tn(v, k * t["tail"])
    return o, st1


def _chunk_bwd(q, k, v, g, st0, do, dst1, fw):
    t = _chunk_terms(q, k, g, fw)
    qcat, km, e = t["qcat"], t["km"], t["e"]
    a_t = jnp.where(t["causal_t"], _dot_nt(km, qcat), 0.0)
    ktail = k * t["tail"]
    dv = _dot(a_t, do) + _dot_nt(ktail, dst1)
    da = jnp.where(t["causal"], _dot_nt(do, v), 0.0)
    da_t = jnp.where(t["causal_t"], _dot_nt(v, do), 0.0)
    dqcat = _dot(da, km)
    dq_inter = e[0] * _dot(do, st0)
    dq = dq_inter
    for j in range(NSUB):
        dq = dq + e[j] * dqcat[:, j * HD:(j + 1) * HD]
    dkm = _dot(da_t, qcat)
    dkt = jnp.zeros_like(k)
    for j in range(NSUB):
        dkt = dkt + jnp.where(t["in_blk"][j], dkm[:, j * HD:(j + 1) * HD], 0.0)
    dk_inter = _dot(v, dst1) * t["tail"]
    dk = dkt * t["kscale"] + dk_inter
    dcum = q * dq_inter - k * dk_inter
    for j in range(NSUB):
        sl = slice(j * HD, (j + 1) * HD)
        dcum = dcum + qcat[:, sl].astype(F32) * dqcat[:, sl] - km[:, sl].astype(F32) * dkm[:, sl]
    ecend = jnp.exp(t["cend"])
    end = ecend * _colsum(st0 * dst1) + _colsum(k * dk_inter)
    dg = _running_sum(dcum, not fw) + end
    dst0 = dst1 * ecend + _dot_tn(do, q * e[0])
    return dq, dk, dv, dg, dst0


def _chunk_index(step, n_ctx_chunks, n_chunks, fw):
    if fw:
        return step
    return jnp.where(step < n_ctx_chunks, n_ctx_chunks - 1 - step, n_chunks - 1 + n_ctx_chunks - step)


def _hg_inputs(hq, hf, lbv, d_idx, sl):
    lb = _sigmoid(lbv[d_idx:d_idx + 1, sl] - lbv[2 + d_idx:3 + d_idx, sl])
    sg = _sigmoid(hf)
    f = lb + (1.0 - lb) * sg
    return _silu(hq), 1.0 - f, jnp.log(f), f, sg, lb


def _scan_fwd(p, side, n_ctx_chunks, fw, branch, name):
    rows = p.shape[0]
    n_chunks = rows // CHUNK
    d_idx = 0 if fw else 1
    hg = branch == "hg"
    cols = (C_HQ, C_HI, C_HF_FW + d_idx) if hg else (C_GQ, C_GK, C_GV)

    def body(*refs):
        if hg:
            a_ref, b_ref, c_ref, lb_ref, o_ref, st_ref, state = refs
        else:
            a_ref, b_ref, c_ref, lr_ref, wgk_ref, bgk_ref, o_ref, st_ref, state = refs
            logits = _dot(lr_ref[...], wgk_ref[...]) + bgk_ref[...]
            g_all = _log_sigmoid(logits) * (1.0 / GATE_NORM)

        @pl.when(pl.program_id(0) == 0)
        def _():
            state[...] = jnp.zeros_like(state)

        for h in range(NH):
            sl = slice(h * HD, (h + 1) * HD)
            if hg:
                q, k, g, _, _, _ = _hg_inputs(a_ref[:, sl], c_ref[:, sl], lb_ref[...], d_idx, sl)
                v = b_ref[:, sl]
            else:
                q, k, v, g = a_ref[:, sl] * (HD ** -0.5), b_ref[:, sl], c_ref[:, sl], g_all[:, sl]
            st0 = state[h]
            st_ref[0, h] = st0
            o, st1 = _chunk_fwd(q, k, v, g, st0, fw)
            o_ref[:, sl] = o
            state[h] = st1

    def cmap(blk):
        return pl.BlockSpec((CHUNK, HW), lambda j: (_chunk_index(j, n_ctx_chunks, n_chunks, fw), blk))

    fixed = lambda j: (0, 0)
    in_specs = [cmap(cols[0]), cmap(cols[1]), cmap(cols[2])]
    if hg:
        in_specs += [pl.BlockSpec((4, HW), fixed)]
        args = (p, p, p, side)
    else:
        in_specs += [pl.BlockSpec((CHUNK, 128), lambda j: (_chunk_index(j, n_ctx_chunks, n_chunks, fw), OFF_LR // 128)),
                     pl.BlockSpec((128, HW), fixed), pl.BlockSpec((1, HW), fixed)]
        args = (p, p, p, p, side[0], side[1])
    return pl.pallas_call(
        body, name=name, grid=(n_chunks,),
        out_shape=[jax.ShapeDtypeStruct((rows, HW), F32), jax.ShapeDtypeStruct((n_chunks, NH, HD, HD), F32)],
        in_specs=in_specs,
        out_specs=[pl.BlockSpec((CHUNK, HW), lambda j: (_chunk_index(j, n_ctx_chunks, n_chunks, fw), 0)),
                   pl.BlockSpec((1, NH, HD, HD), lambda j: (_chunk_index(j, n_ctx_chunks, n_chunks, fw), 0, 0, 0))],
        scratch_shapes=[pltpu.VMEM((NH, HD, HD), F32)],
        compiler_params=_cparams(dimension_semantics=("arbitrary",)),
    )(*args)


def _scan_fwd_both(p, side, n_ctx_chunks, branch, name):
    rows = p.shape[0]
    n_chunks = rows // CHUNK
    hg = branch == "hg"
    n_in = 4 if hg else 6

    def body(*refs):
        ins, outs, state = refs[:2 * n_in], refs[2 * n_in:2 * n_in + 4], refs[-1]

        @pl.when(pl.program_id(0) == 0)
        def _():
            state[...] = jnp.zeros_like(state)

        for di, fw in enumerate((True, False)):
            r = ins[di * n_in:(di + 1) * n_in]
            o_ref, st_ref = outs[2 * di], outs[2 * di + 1]
            if hg:
                a_ref, b_ref, c_ref, lb_ref = r
            else:
                a_ref, b_ref, c_ref, lr_ref, wgk_ref, bgk_ref = r
                logits = _dot(lr_ref[...], wgk_ref[...]) + bgk_ref[...]
                g_all = _log_sigmoid(logits) * (1.0 / GATE_NORM)
            for h in range(NH):
                sl = slice(h * HD, (h + 1) * HD)
                if hg:
                    q, k, g, _, _, _ = _hg_inputs(a_ref[:, sl], c_ref[:, sl], lb_ref[...], di, sl)
                    v = b_ref[:, sl]
                else:
                    q, k, v, g = a_ref[:, sl] * (HD ** -0.5), b_ref[:, sl], c_ref[:, sl], g_all[:, sl]
                st0 = state[di, h]
                st_ref[0, h] = st0
                o, st1 = _chunk_fwd(q, k, v, g, st0, fw)
                o_ref[:, sl] = o
                state[di, h] = st1

    fixed = lambda j: (0, 0)
    in_specs, args, out_specs = [], [], []
    for di, fw in enumerate((True, False)):
        chunk = functools.partial(_chunk_index, n_ctx_chunks=n_ctx_chunks, n_chunks=n_chunks, fw=fw)

        def cmap(blk, width=HW, chunk=chunk):
            return pl.BlockSpec((CHUNK, width), lambda j: (chunk(j), blk))

        if hg:
            in_specs += [cmap(C_HQ), cmap(C_HI), cmap(C_HF_FW + di), pl.BlockSpec((4, HW), fixed)]
            args += [p, p, p, side]
        else:
            in_specs += [cmap(C_GQ), cmap(C_GK), cmap(C_GV), cmap(OFF_LR // 128, 128),
                         pl.BlockSpec((128, HW), fixed), pl.BlockSpec((1, HW), fixed)]
            args += [p, p, p, p, side[di][0], side[di][1]]
        out_specs += [cmap(0), pl.BlockSpec((1, NH, HD, HD), lambda j, chunk=chunk: (chunk(j), 0, 0, 0))]
    return pl.pallas_call(
        body, name=name, grid=(n_chunks,),
        out_shape=[jax.ShapeDtypeStruct((rows, HW), F32), jax.ShapeDtypeStruct((n_chunks, NH, HD, HD), F32)] * 2,
        in_specs=in_specs, out_specs=out_specs,
        scratch_shapes=[pltpu.VMEM((2, NH, HD, HD), F32)],
        compiler_params=_cparams(dimension_semantics=("arbitrary",)),
    )(*args)


def _scan_bwd(p, side, states, d_o, n_ctx_chunks, fw, branch, name):
    rows = p.shape[0]
    n_chunks = rows // CHUNK
    d_idx = 0 if fw else 1
    hg = branch == "hg"
    cols = (C_HQ, C_HI, C_HF_FW + d_idx) if hg else (C_GQ, C_GK, C_GV)

    def body(*refs):
        if hg:
            a_ref, b_ref, c_ref, lb_ref, st_ref, do_ref, da_ref, db_ref, dc_ref, dlb_ref, dstate = refs
        else:
            (a_ref, b_ref, c_ref, lr_ref, wgk_ref, bgk_ref, st_ref, do_ref, da_ref, db_ref, dc_ref, dlr_ref,
             dwgk_ref, dbias_ref, dstate) = refs
            lr = lr_ref[...]
            logits = _dot(lr, wgk_ref[...]) + bgk_ref[...]
            g_all = _log_sigmoid(logits) * (1.0 / GATE_NORM)

        @pl.when(pl.program_id(0) == 0)
        def _():
            dstate[...] = jnp.zeros_like(dstate)
            if hg:
                dlb_ref[...] = jnp.zeros_like(dlb_ref)
            else:
                dwgk_ref[...] = jnp.zeros_like(dwgk_ref)
                dbias_ref[...] = jnp.zeros_like(dbias_ref)

        dg_parts = []
        for h in range(NH):
            sl = slice(h * HD, (h + 1) * HD)
            if hg:
                hq, hf = a_ref[:, sl], c_ref[:, sl]
                q, k, g, f, sg, lb = _hg_inputs(hq, hf, lb_ref[...], d_idx, sl)
                v = b_ref[:, sl]
            else:
                q, k, v, g = a_ref[:, sl] * (HD ** -0.5), b_ref[:, sl], c_ref[:, sl], g_all[:, sl]
            dq, dk, dv, dg, dst0 = _chunk_bwd(q, k, v, g, st_ref[0, h], do_ref[:, sl], dstate[h], fw)
            dstate[h] = dst0
            if hg:
                da_ref[:, sl] = dq * _dsilu(hq)
                db_ref[:, sl] = dv
                df = dg / f - dk
                dc_ref[:, sl] = df * (1.0 - lb) * sg * (1.0 - sg)
                dlb_ref[0:1, sl] += _colsum(df * (1.0 - sg))
            else:
                da_ref[:, sl] = dq * (HD ** -0.5)
                db_ref[:, sl] = dk
                dc_ref[:, sl] = dv
                dg_parts.append(dg)
        if not hg:
            dlogits = jnp.concatenate(dg_parts, axis=1) * (1.0 / GATE_NORM) * (1.0 - _sigmoid(logits))
            dlr_ref[...] = _dot_nt(dlogits, wgk_ref[...])
            dwgk_ref[...] += _dot_tn(lr, dlogits)
            dbias_ref[0:1, :] += _colsum(dlogits)

    def chunk_of(j):
        return _chunk_index(n_chunks - 1 - j, n_ctx_chunks, n_chunks, fw)

    def cmap(blk, width=HW):
        return pl.BlockSpec((CHUNK, width), lambda j: (chunk_of(j), blk))

    fixed = lambda j: (0, 0)
    st_spec = pl.BlockSpec((1, NH, HD, HD), lambda j: (chunk_of(j), 0, 0, 0))
    big = jax.ShapeDtypeStruct((rows, HW), F32)
    if hg:
        in_specs = [cmap(cols[0]), cmap(cols[1]), cmap(cols[2]), pl.BlockSpec((4, HW), fixed), st_spec, cmap(0)]
        args = (p, p, p, side, states, d_o)
        out_shape = [big, big, big, jax.ShapeDtypeStruct((8, HW), F32)]
        out_specs = [cmap(0), cmap(0), cmap(0), pl.BlockSpec((8, HW), fixed)]
    else:
        in_specs = [cmap(cols[0]), cmap(cols[1]), cmap(cols[2]), cmap(OFF_LR // 128, 128),
                    pl.BlockSpec((128, HW), fixed), pl.BlockSpec((1, HW), fixed), st_spec, cmap(0)]
        args = (p, p, p, p, side[0], side[1], states, d_o)
        out_shape = [big, big, big, jax.ShapeDtypeStruct((rows, 128), F32), jax.ShapeDtypeStruct((128, HW), F32),
                     jax.ShapeDtypeStruct((8, HW), F32)]
        out_specs = [cmap(0), cmap(0), cmap(0), cmap(0, 128), pl.BlockSpec((128, HW), fixed),
                     pl.BlockSpec((8, HW), fixed)]
    return pl.pallas_call(
        body, name=name, grid=(n_chunks,), out_shape=out_shape, in_specs=in_specs, out_specs=out_specs,
        scratch_shapes=[pltpu.VMEM((NH, HD, HD), F32)],
        compiler_params=_cparams(dimension_semantics=("arbitrary",)),
    )(*args)


SMALL_ROWS = 56
ROWS_MOD_X = (0, 1, 8, 16, 17, 18)
ROWS_MOD_C = (2, 3)
ROW_PRE1, ROW_POST1, ROW_ONORM, ROW_PRE2, ROW_POST2, ROW_LB, ROW_BGK, ROW_WGK = 4, 9, 10, 19, 20, 24, 32, 40
ROW_LOSS = 21


def _reduce_small(gathered, lb_full, name):
    _, _, d = gathered.shape

    def body(g_ref, lb_ref, sum_ref, dmod_ref, dbmod_ref, dlb_ref):
        total = g_ref[0]
        for b in range(1, N_DEV):
            total = total + g_ref[b]
        sum_ref[...] = total
        dmod_ref[...] = jnp.zeros_like(dmod_ref)
        for m in range(N_MOD):
            col = slice(m * d, (m + 1) * d)
            acc = jnp.zeros((1, d), F32)
            for b in range(N_DEV):
                row = g_ref[b, ROWS_MOD_X[m]:ROWS_MOD_X[m] + 1, :]
                dmod_ref[b:b + 1, col] = row
                acc = acc + row
            if m < 2:
                ctx_row = total[ROWS_MOD_C[m]:ROWS_MOD_C[m] + 1, :]
                dmod_ref[8:9, col] = ctx_row
                acc = acc + ctx_row
            dbmod_ref[:, col] = acc
        lbv = lb_ref[...]
        for dd in range(2):
            lb = _sigmoid(lbv[dd:dd + 1, :] - lbv[2 + dd:3 + dd, :])
            gl = total[ROW_LB:ROW_LB + 1, dd * HW:(dd + 1) * HW] * lb * (1.0 - lb)
            dlb_ref[dd:dd + 1, :] = gl
            dlb_ref[2 + dd:3 + dd, :] = -gl

    return pl.pallas_call(
        body, name=name,
        out_shape=[jax.ShapeDtypeStruct((SMALL_ROWS, d), F32), jax.ShapeDtypeStruct((16, N_MOD * d), F32),
                   jax.ShapeDtypeStruct((1, N_MOD * d), F32), jax.ShapeDtypeStruct((4, HW), F32)],
        in_specs=[VMEM_SPEC] * 2, out_specs=[VMEM_SPEC] * 4, compiler_params=_cparams(),
    )(gathered, lb_full)


def _c_ctx_grad(gathered, c_ctx_row, name):
    def body(g_ref, c_ref, o_ref):
        acc = g_ref[0, 0:1, :]
        for chip in range(1, N_CHIP):
            acc = acc + g_ref[2 * chip, 0:1, :]
        o_ref[...] = acc * _dsilu(c_ref[...])

    return pl.pallas_call(
        body, name=name, out_shape=jax.ShapeDtypeStruct(c_ctx_row.shape, F32),
        in_specs=[VMEM_SPEC] * 2, out_specs=VMEM_SPEC, compiler_params=_cparams(),
    )(gathered, c_ctx_row)


def _relayout_w_in(w):
    pad = jnp.zeros((w.shape[0], 128 - 2 * RANK), w.dtype)
    return jnp.concatenate([w[:, :9 * HW], w[:, 9 * HW + 2 * RANK:], w[:, 9 * HW:9 * HW + 2 * RANK], pad], axis=1)


def _relayout_w_in_rows(wt):
    pad = jnp.zeros((128 - 2 * RANK, wt.shape[1]), wt.dtype)
    return jnp.concatenate([wt[:9 * HW], wt[9 * HW + 2 * RANK:], wt[9 * HW:9 * HW + 2 * RANK], pad], axis=0)


def _w_in_grad_rows(g_main, g_lr):
    return jnp.concatenate([g_main[:9 * HW], g_lr[:2 * RANK], g_main[9 * HW:]], axis=0)


def _w_in_grad_blocks(g_main, g_lr, n_blocks):
    lr0 = 9 * HW
    n = (g_main.shape[1] + 2 * RANK) // n_blocks

    def cols(lo, hi):
        out = []
        if lo < lr0:
            out.append(g_main[:, lo:min(hi, lr0)])
        if hi > lr0 and lo < lr0 + 2 * RANK:
            out.append(g_lr[:, max(lo, lr0) - lr0:min(hi, lr0 + 2 * RANK) - lr0])
        if hi > lr0 + 2 * RANK:
            out.append(g_main[:, max(lo, lr0 + 2 * RANK) - 2 * RANK:hi - 2 * RANK])
        return out

    return jnp.stack([jnp.concatenate(cols(j * n, (j + 1) * n), axis=1) for j in range(n_blocks)])


def _blocked(full, n_blocks):
    k, n = full.shape
    return full.reshape(k, n_blocks, n // n_blocks).transpose(1, 0, 2)


def _unblocked(blocks):
    nb, k, n = blocks.shape
    return blocks.transpose(1, 0, 2).reshape(k, nb * n)


def _sample_front(x0, ctx0, modc, modx, norm_pre1, lb_full, gla_side, w_in_r):
    ctx_len = ctx0.shape[0]
    n_ctx_tiles = ctx_len // TM
    n_ctx_chunks = ctx_len // CHUNK
    h1, p = _in_projection(ctx0, x0, modc, modx, norm_pre1, w_in_r, n_ctx_tiles, "in_projection")
    o_hg_fw, st_hg_fw, o_hg_bw, st_hg_bw = _scan_fwd_both(p, lb_full, n_ctx_chunks, "hg", "scan_hg")
    o_gla_fw, st_gla_fw, o_gla_bw, st_gla_bw = _scan_fwd_both(p, gla_side, n_ctx_chunks, "gla", "scan_gla")
    return dict(h1=h1, p=p, o_list=[o_hg_fw, o_hg_bw, o_gla_fw, o_gla_bw],
                states=[st_hg_fw, st_hg_bw, st_gla_fw, st_gla_bw])


def _sample_back(reduce, front, x0, ctx0, target0, modc, modx, norm_pre1, norms, onorms, lb_full, gla_side, w_in_r,
                 wbh, wbg, wout, wg, wu, wd):
    seq, d = x0.shape
    ctx_len = ctx0.shape[0]
    n_ctx_tiles = ctx_len // TM
    n_tiles = (ctx_len + seq) // TM
    n_ctx_chunks = ctx_len // CHUNK
    h1, p, o_list = front["h1"], front["p"], front["o_list"]
    st_hg_fw, st_hg_bw, st_gla_fw, st_gla_bw = front["states"]
    z2, y1, merged, og_hg, og_gla = _mixer_tail_fwd(x0, p, o_list, modx, norms, onorms, wbh, wbg, wout, n_ctx_tiles,
                                                    "mixer_tail")
    loss_part, dz2, h2, a_act, du, dv, dy2, stat_ffn = _ffn_fwd_bwd(z2, modx, norms, wg, wu, wd, target0, "ffn")
    dff = wg.shape[0]
    tok = reduce("ffn", [_weight_grad(du, h2, "grad_w_ff_gate", tk=dff // 2, tn=d),
                         _weight_grad(dv, h2, "grad_w_ff_up", tk=dff // 2, tn=d),
                         _weight_grad(a_act, dy2, "grad_w_ff_down", tk=dff // 2)])

    (d_ohg, d_ogla, d_hgate, d_ggate, d_ghg, d_ggla, dy1, db_hg, db_gla, stat_mix) = _mixer_tail_bwd(
        x0, p, o_list, dz2, y1, modx + tok, norms, onorms, wbh, wbg, wout, n_ctx_tiles, n_tiles, "mixer_tail_bwd")
    tok = reduce("mix", [_weight_grad(og_hg, db_hg, "grad_w_br_hg"), _weight_grad(og_gla, db_gla, "grad_w_br_gla"),
                         _weight_grad(merged, dy1, "grad_w_out")])
    lb_b = lb_full + tok
    gla_b = [(wgk, bias + tok) for wgk, bias in gla_side]
    dhq_f, dhi_f, dhf_f, dlb_f = _scan_bwd(p, lb_b, st_hg_fw, d_ohg, n_ctx_chunks, True, "hg", "scan_hg_fw_bwd")
    dhq_b, dhi_b, dhf_b, dlb_b = _scan_bwd(p, lb_b, st_hg_bw, d_ohg, n_ctx_chunks, False, "hg", "scan_hg_bw_bwd")
    dgq_f, dgk_f, dgv_f, dlr_f, dwgk_f, dbgk_f = _scan_bwd(p, gla_b[0], st_gla_fw, d_ogla, n_ctx_chunks, True, "gla",
                                                           "scan_gla_fw_bwd")
    dgq_b, dgk_b, dgv_b, dlr_b, dwgk_b, dbgk_b = _scan_bwd(p, gla_b[1], st_gla_bw, d_ogla, n_ctx_chunks, False, "gla",
                                                           "scan_gla_bw_bwd")
    pieces = [dhq_f, dhq_b, dhi_f, dhi_b, dhf_f, dhf_b, d_hgate, dgq_f, dgq_b, dgk_f, dgk_b, dgv_f, dgv_b, d_ggate,
              d_ghg, d_ggla, dlr_f, dlr_b]
    dp, grad_x, stat_in = _in_projection_bwd(ctx0, x0, dz2, modc, modx, norm_pre1, w_in_r, pieces, n_ctx_tiles,
                                             "in_projection_bwd")

    reduce("small", dict(stat_in=stat_in, stat_mix=stat_mix, stat_ffn=stat_ffn, dlb=(dlb_f, dlb_b),
                         dwgk=(dwgk_f, dwgk_b), dbgk=(dbgk_f, dbgk_b)))
    reduce("in", [_w_in_grad(dp, h1, w_in_r.shape[0], "grad_w_in")])
    return dict(
        loss_part=loss_part, grad_x=grad_x, stat_in=stat_in, stat_mix=stat_mix, stat_ffn=stat_ffn,
        dlb=(dlb_f, dlb_b), dwgk=(dwgk_f, dwgk_b), dbgk=(dbgk_f, dbgk_b))


def kernel(x, c, ctx, c_ctx, w_mod, b_mod, norm_pre1, norm_post1, norm_pre2, norm_post2, w_in, hg_lb, hg_onorm, gla_w_gk, gla_b_gk, gla_onorm, w_br_hg, w_br_gla, w_out, w_ff_gate, w_ff_up, w_ff_down, loss_target, m_c_ctx, m_w_mod, m_b_mod, m_norm_pre1, m_norm_post1, m_norm_pre2, m_norm_post2, m_w_in, m_hg_lb, m_hg_onorm, m_gla_w_gk, m_gla_b_gk, m_gla_onorm, m_w_br_hg, m_w_br_gla, m_w_out, m_w_ff_gate, m_w_ff_up, m_w_ff_down, v_c_ctx, v_w_mod, v_b_mod, v_norm_pre1, v_norm_post1, v_norm_pre2, v_norm_post2, v_w_in, v_hg_lb, v_hg_onorm, v_gla_w_gk, v_gla_b_gk, v_gla_onorm, v_w_br_hg, v_w_br_gla, v_w_out, v_w_ff_gate, v_w_ff_up, v_w_ff_down):
    seq, d = x.shape[1], x.shape[2]
    ctx_len = ctx.shape[1]
    assert seq % TM == 0 and ctx_len % TM == 0 and d == 2 * HW
    ax, ay, ac = lax.axis_index("x"), lax.axis_index("y"), lax.axis_index("c")
    chip = 2 * ax + ay
    dev = 2 * chip + ac
    c_arr = jnp.reshape(ac, (1,)).astype(jnp.int32)

    nc = d // 128
    pad8 = lambda a: jnp.pad(a, ((0, -a.shape[0] % 8), (0, 0)))
    small1 = jnp.concatenate([c.reshape(nc, 128), pad8(hg_lb.reshape(4, 128)), gla_w_gk.reshape(2 * RANK, 128),
                              pad8(gla_b_gk.reshape(2, 128))], axis=0)
    got1 = _allgather8(small1, "gather_small_params")
    c_all = got1[:, :nc, :].reshape(N_DEV, d)
    per_chip = got1[0::2]
    lb_full = per_chip[:, nc:nc + 4, :].transpose(1, 0, 2).reshape(4, HW)
    wgk_full = per_chip[:, nc + 8:nc + 8 + 2 * RANK, :].transpose(1, 0, 2).reshape(2, RANK, HW)
    bgk_full = per_chip[:, nc + 8 + 2 * RANK:nc + 10 + 2 * RANK, :].transpose(1, 0, 2).reshape(2, HW)
    wgk_pad = [jnp.zeros((128, HW), F32).at[dd * RANK:(dd + 1) * RANK].set(wgk_full[dd]) for dd in range(2)]
    bgk = [bgk_full[dd:dd + 1] for dd in range(2)]

    n_mod_cols = w_mod.shape[2]
    cond = jnp.concatenate([c_all, pad8(c_ctx.reshape(1, d))], axis=0)
    b_cols = lax.dynamic_slice(b_mod, (0, chip * n_mod_cols), (1, n_mod_cols))
    mod_part = _mod_forward(cond, w_mod[0], b_cols, "mod_forward")
    mod_got = _allgather8(mod_part, "gather_mod")
    mod_all = mod_got[0::2].transpose(1, 0, 2).reshape(16, N_CHIP * n_mod_cols)
    modx = pad8(lax.dynamic_slice(mod_all, (dev, 0), (1, N_MOD * d)).reshape(N_MOD, d))
    modc = pad8(mod_all[8].reshape(N_MOD, d))

    chip_arr = jnp.reshape(chip, (1,)).astype(jnp.int32)
    transposed = ("w_in", "w_ff_gate", "w_ff_up")
    view = lambda a, nm: a[0].T if nm in transposed else a[0]
    blocks = [_cast_into_blocks(chip_arr, view(w_, nm), "cast_" + nm) for w_, nm in (
        (w_in, "w_in"), (w_br_hg, "w_br_hg"), (w_br_gla, "w_br_gla"), (w_out, "w_out"), (w_ff_gate, "w_ff_gate"),
        (w_ff_up, "w_ff_up"), (w_ff_down, "w_ff_down"))]
    gathered_in = _gather_blocks(blocks[:1], "gather_w_in", after=[mod_got])
    sems, lands, token = _blocks_start(blocks[1:], "gather_rest_start", after=[gathered_in[0]])
    w_in_r = gathered_in[0].reshape(-1, d)

    norms = jnp.concatenate([norm_pre1, norm_post1, norm_pre2, norm_post2, jnp.zeros((4, d), F32)], axis=0)
    onorms = jnp.zeros((8, d), F32).at[0, :HD].set(hg_onorm[0]).at[1, :HD].set(gla_onorm[0])
    gla_side = [(wgk_pad[dd], bgk[dd]) for dd in range(2)]
    modx = modx + token[0, 0]
    front = _sample_front(x[0], ctx[0], modc, modx, norm_pre1, lb_full, gla_side, w_in_r)
    lands = _blocks_wait(sems, lands, front["o_list"], "gather_rest_wait")
    gathered = _blocks_finish(lands, "gather_rest_finish")
    wbh, wbg = _unblocked(gathered[0]), _unblocked(gathered[1])
    wout = gathered[2].reshape(d, d)
    wg, wu, wd = (gathered[i].reshape(-1, d) for i in (3, 4, 5))
    dff = wg.shape[0]
    groups = {"ffn": ["w_ff_gate", "w_ff_up", "w_ff_down"], "mix": ["w_br_hg", "w_br_gla", "w_out"], "in": ["w_in"]}
    row_sharded = {"w_out": d // N_CHIP, "w_ff_down": dff // N_CHIP, "w_ff_gate": dff // N_CHIP,
                   "w_ff_up": dff // N_CHIP, "w_in": w_in.shape[2]}
    in_flight = {}

    small = {}

    def reduce_small(stats):
        small2 = jnp.concatenate([
            stats["stat_in"], stats["stat_mix"], stats["stat_ffn"],
            jnp.concatenate(stats["dlb"], axis=1), jnp.concatenate(stats["dbgk"], axis=1),
            jnp.concatenate([stats["dwgk"][0][0:RANK], stats["dwgk"][1][RANK:2 * RANK]], axis=1)], axis=0)
        assert small2.shape[0] == SMALL_ROWS
        got2 = _allgather8(small2, "gather_small_grads")
        total, dmod_all, g_b_mod, g_lb_full = _reduce_small(got2, lb_full, "reduce_small")
        dmod_cols = lax.dynamic_slice(dmod_all, (0, chip * n_mod_cols), (16, n_mod_cols))
        g_w_mod, cctx_part = _mod_backward(cond, w_mod[0], dmod_cols, "mod_backward")
        got3 = _allgather8(cctx_part, "gather_c_ctx_grad")
        g_c_ctx = _c_ctx_grad(got3, c_ctx.reshape(1, d), "c_ctx_grad")
        small.update(total=total, g_b_mod=g_b_mod, g_lb_full=g_lb_full, g_w_mod=g_w_mod, g_c_ctx=g_c_ctx)

    def reduce(group, grads):
        if group == "small":
            return reduce_small(grads)
        nms = groups[group]
        full = [g.reshape(N_CHIP, row_sharded[nm], d) if nm in row_sharded else _blocked(g, N_CHIP)
                for g, nm in zip(grads, nms)]
        from_sibling = _send_other_half(full, "grads_to_sibling_" + group)
        pairs = [_pair_sum(c_arr, f, r_, "pair_sum_" + nm) for f, r_, nm in zip(full, from_sibling, nms)]
        after = [small["g_c_ctx"], small["total"]] if group == "in" else []
        sems_, pairs, lands_, token_ = _scatter_start(pairs, "grads_to_owner_start_" + group, after)
        in_flight[group] = (sems_, pairs, lands_, token_)
        return token_[0, 0]

    r = _sample_back(reduce, front, x[0], ctx[0], loss_target[0], modc, modx, norm_pre1, norms, onorms, lb_full,
                     gla_side, w_in_r, wbh, wbg, wout, wg, wu, wd)
    loss_part, grad_x, stat_in, stat_mix, stat_ffn = (r[k] for k in ("loss_part", "grad_x", "stat_in", "stat_mix",
                                                                     "stat_ffn"))
    (dlb_f, dlb_b), (dwgk_f, dwgk_b), (dbgk_f, dbgk_b) = r["dlb"], r["dwgk"], r["dbgk"]

    weights = dict(w_in=(w_in, m_w_in, v_w_in), w_br_hg=(w_br_hg, m_w_br_hg, v_w_br_hg),
                   w_br_gla=(w_br_gla, m_w_br_gla, v_w_br_gla), w_out=(w_out, m_w_out, v_w_out),
                   w_ff_gate=(w_ff_gate, m_w_ff_gate, v_w_ff_gate), w_ff_up=(w_ff_up, m_w_ff_up, v_w_ff_up),
                   w_ff_down=(w_ff_down, m_w_ff_down, v_w_ff_down))
    names = ["w_in", "w_br_hg", "w_br_gla", "w_out", "w_ff_gate", "w_ff_up", "w_ff_down"]
    big = {}

    def finish(group, after):
        sems_, pairs, lands_, _ = in_flight[group]
        pairs, lands_ = _scatter_wait(sems_, pairs, lands_, after, "grads_to_owner_wait_" + group)
        own_half = [_sum_owner(chip_arr, pr, g, "chip_sum_" + nm) for pr, g, nm in zip(pairs, lands_, groups[group])]
        other_half = _swap_with_sibling(own_half, "halves_to_sibling_" + group)
        for nm, own, oth in zip(groups[group], own_half, other_half):
            w_, m_, v_ = (view(a, nm) for a in weights[nm])
            res = _adamw_halves(c_arr, own, oth, w_, m_, v_, "adamw_" + nm)
            big[nm] = [r_.T[None] if nm in transposed else r_[None] for r_ in res]
        return big[groups[group][-1]][1]

    token_in = in_flight["in"][3]
    done_ffn = finish("ffn", [token_in])
    done_mix = finish("mix", [done_ffn])

    total, g_b_mod, g_lb_full, g_w_mod, g_c_ctx = (small[k] for k in ("total", "g_b_mod", "g_lb_full", "g_w_mod",
                                                                      "g_c_ctx"))
    g_pre1, g_post1, g_pre2, g_post2 = (total[r_:r_ + 1] for r_ in (ROW_PRE1, ROW_POST1, ROW_PRE2, ROW_POST2))
    g_hg_on, g_gla_on = total[ROW_ONORM:ROW_ONORM + 1, 0:HD], total[ROW_ONORM:ROW_ONORM + 1, HD:2 * HD]
    n_lb = hg_lb.shape[2]
    g_hg_lb = lax.dynamic_slice(g_lb_full, (0, chip * n_lb), (4, n_lb))
    g_bgk = lax.dynamic_slice(total[ROW_BGK:ROW_BGK + 1].reshape(2, HW), (0, chip * n_lb), (2, n_lb))
    g_wgk_full = total[ROW_WGK:ROW_WGK + RANK].reshape(RANK, 2, HW).transpose(1, 0, 2).reshape(2 * RANK, HW)
    g_wgk = lax.dynamic_slice(g_wgk_full, (0, chip * n_lb), (2 * RANK, n_lb))

    small_items = [
        (g_c_ctx, c_ctx.reshape(1, d), m_c_ctx.reshape(1, d), v_c_ctx.reshape(1, d)),
        (g_b_mod, b_mod, m_b_mod, v_b_mod),
        (g_pre1, norm_pre1, m_norm_pre1, v_norm_pre1),
        (g_post1, norm_post1, m_norm_post1, v_norm_post1),
        (g_pre2, norm_pre2, m_norm_pre2, v_norm_pre2),
        (g_post2, norm_post2, m_norm_post2, v_norm_post2),
        (g_hg_lb, hg_lb.reshape(4, n_lb), m_hg_lb.reshape(4, n_lb), v_hg_lb.reshape(4, n_lb)),
        (g_hg_on, hg_onorm, m_hg_onorm, v_hg_onorm),
        (g_wgk, gla_w_gk.reshape(2 * RANK, n_lb), m_gla_w_gk.reshape(2 * RANK, n_lb), v_gla_w_gk.reshape(2 * RANK, n_lb)),
        (g_bgk, gla_b_gk.reshape(2, n_lb), m_gla_b_gk.reshape(2, n_lb), v_gla_b_gk.reshape(2, n_lb)),
        (g_gla_on, gla_onorm, m_gla_onorm, v_gla_onorm),
    ]
    small_res = _adamw_whole(small_items, "adamw_small")
    mod_res = _adamw_tiled(g_w_mod, w_mod[0], m_w_mod[0], v_w_mod[0], "adamw_w_mod")
    finish("in", [done_mix, mod_res[0], small_res[0][0]])

    loss = total[ROW_LOSS, 0]

    shapes = dict(c_ctx=c_ctx.shape, b_mod=b_mod.shape, norm_pre1=norm_pre1.shape, norm_post1=norm_post1.shape,
                  norm_pre2=norm_pre2.shape, norm_post2=norm_post2.shape, hg_lb=hg_lb.shape, hg_onorm=hg_onorm.shape,
                  gla_w_gk=gla_w_gk.shape, gla_b_gk=gla_b_gk.shape, gla_onorm=gla_onorm.shape)
    small_names = ["c_ctx", "b_mod", "norm_pre1", "norm_post1", "norm_pre2", "norm_post2", "hg_lb", "hg_onorm",
                   "gla_w_gk", "gla_b_gk", "gla_onorm"]
    grads, deltas, new_m, new_v = {}, {}, {}, {}
    for nm, item, res in zip(small_names, small_items, small_res):
        grads[nm] = item[0].reshape(shapes[nm])
        deltas[nm], new_m[nm], new_v[nm] = (r.reshape(shapes[nm]) for r in res)
    grads["w_mod"] = g_w_mod[None]
    deltas["w_mod"], new_m["w_mod"], new_v["w_mod"] = (r[None] for r in mod_res)
    for nm in names:
        grads[nm], deltas[nm], new_m[nm], new_v[nm] = big[nm]
    order = ["c_ctx", "w_mod", "b_mod", "norm_pre1", "norm_post1", "norm_pre2", "norm_post2", "w_in", "hg_lb",
             "hg_onorm", "gla_w_gk", "gla_b_gk", "gla_onorm", "w_br_hg", "w_br_gla", "w_out", "w_ff_gate", "w_ff_up",
             "w_ff_down"]
    return (loss, grad_x[None], *[grads[n] for n in order], *[deltas[n] for n in order],
            *[new_m[n] for n in order], *[new_v[n] for n in order])


def _weight_grad_cols(xs, dy, n_cols, name, tn=512):
    rows = dy.shape[0]
    k = tk = xs.shape[1]

    return pl.pallas_call(
        functools.partial(_transposed_lhs_matmul), name=name, grid=(k // tk, n_cols // tn),
        out_shape=jax.ShapeDtypeStruct((k, n_cols), F32),
        in_specs=[pl.BlockSpec((rows, tk), lambda i, j: (0, i)), pl.BlockSpec((rows, tn), lambda i, j: (0, j))],
        out_specs=pl.BlockSpec((tk, tn), lambda i, j: (i, j)),
        scratch_shapes=[pltpu.VMEM((tk, rows), BF16)],
        compiler_params=_cparams(dimension_semantics=("parallel", "arbitrary")),
    )(xs, dy)
```

```python
import functools

import jax
import jax.numpy as jnp
from jax import lax
from jax.experimental import pallas as pl
from jax.experimental.pallas import tpu as pltpu

F32 = jnp.float32
BF16 = jnp.bfloat16
HIGHEST = lax.Precision.HIGHEST
MESH = pl.DeviceIdType.MESH

EPS = 1e-6
CHUNK = 64
SUB = 16
NSUB = CHUNK // SUB
NH = 4
HD = 128
HW = NH * HD
RANK = 16
GATE_NORM = 16.0
N_MOD = 6
TM = 256
TM_FFN = 128
N_DEV = 8
N_CHIP = 4
VMEM_LIMIT = 56 * 1024 * 1024

ADAM_LR = 0.001
ADAM_B1 = 0.9
ADAM_B2 = 0.999
ADAM_EPS = 1e-08
ADAM_WD = 0.01
ADAM_STEP = 10

VMEM_SPEC = pl.BlockSpec(memory_space=pltpu.VMEM)
ANY_SPEC = pl.BlockSpec(memory_space=pl.ANY)
HBM_SPEC = pl.BlockSpec(memory_space=pltpu.HBM)
SEM_SPEC = pl.BlockSpec(memory_space=pltpu.SEMAPHORE)
EFFECT = pltpu.SideEffectType.DATAFLOW_SIDE_EFFECTING


def _cparams(**kw):
    return pltpu.CompilerParams(vmem_limit_bytes=VMEM_LIMIT, **kw)


def _dot(a, b):
    return jnp.dot(a.astype(BF16), b.astype(BF16), preferred_element_type=F32)


def _dot_nt(a, b):
    return lax.dot_general(a.astype(BF16), b.astype(BF16), (((1,), (1,)), ((), ())), preferred_element_type=F32)


def _dot_tn(a, b):
    return lax.dot_general(a.astype(BF16), b.astype(BF16), (((0,), (0,)), ((), ())), preferred_element_type=F32)


def _sigmoid(x):
    return 1.0 / (1.0 + jnp.exp(-x))


def _silu(x):
    return x * _sigmoid(x)


def _dsilu(x):
    s = _sigmoid(x)
    return s * (1.0 + x * (1.0 - s))


def _log_sigmoid(x):
    return jnp.minimum(x, 0.0) - jnp.log(1.0 + jnp.exp(-jnp.abs(x)))


def _colsum(a):
    return jnp.sum(a, axis=0, keepdims=True)


def _rms(a):
    r = lax.rsqrt(jnp.mean(a * a, axis=-1, keepdims=True) + EPS)
    return a * r, r


def _rms_bwd(dn, n, r):
    return r * (dn - n * jnp.mean(dn * n, axis=-1, keepdims=True))


def _place():
    x, y, c = lax.axis_index("x"), lax.axis_index("y"), lax.axis_index("c")
    chips = [(1 - x, y), (x, 1 - y), (1 - x, 1 - y)]
    return x, y, c, chips


def _allgather8(v, name):
    rows, cols = v.shape

    def body(x_ref, out_ref, send_sems, recv_sems, local_sem):
        x, y, c, chips = _place()
        me, sibling = (x, y, c), (x, y, 1 - c)

        def blk(px, py, pc):
            return out_ref.at[4 * px + 2 * py + pc]

        def copy(k, block, to, src=None):
            return pltpu.make_async_remote_copy(
                src_ref=blk(*block) if src is None else src, dst_ref=blk(*block),
                send_sem=send_sems.at[k], recv_sem=recv_sems.at[k], device_id=to, device_id_type=MESH)

        mine = pltpu.make_async_copy(x_ref, blk(*me), local_sem)
        mine.start()
        first = [copy(0, me, sibling, src=x_ref)]
        first += [copy(1 + j, me, (*chip, c), src=x_ref) for j, chip in enumerate(chips)]
        for cp in first:
            cp.start()
        passed = [copy(4 + j, (*chip, c), sibling) for j, chip in enumerate(chips)]
        for j, chip in enumerate(chips):
            copy(1 + j, (*chip, c), me).wait_recv()
            passed[j].start()
        copy(0, sibling, me).wait_recv()
        for j, chip in enumerate(chips):
            copy(4 + j, (*chip, 1 - c), me).wait_recv()
        for cp in first + passed:
            cp.wait_send()
        mine.wait()

    return pl.pallas_call(
        body, name=name,
        out_shape=jax.ShapeDtypeStruct((N_DEV, rows, cols), v.dtype),
        in_specs=[VMEM_SPEC], out_specs=VMEM_SPEC,
        scratch_shapes=[pltpu.SemaphoreType.DMA((7,)), pltpu.SemaphoreType.DMA((7,)), pltpu.SemaphoreType.DMA],
    )(v)


def _cast_into_blocks(chip_arr, w, name):
    rows, cols = w.shape
    tr = _row_tile(rows, 16, 256)

    def body(chip_ref, w_ref, o_ref):
        o_ref[0] = w_ref[...].astype(BF16)

    return pl.pallas_call(
        body, name=name,
        grid_spec=pltpu.PrefetchScalarGridSpec(
            num_scalar_prefetch=1, grid=(rows // tr,),
            in_specs=[pl.BlockSpec((tr, cols), lambda i, chip_ref: (i, 0))],
            out_specs=pl.BlockSpec((1, tr, cols), lambda i, chip_ref: (chip_ref[0], i, 0))),
        out_shape=jax.ShapeDtypeStruct((N_CHIP, rows, cols), BF16),
        compiler_params=_cparams(dimension_semantics=("parallel",)),
    )(chip_arr, w)


def _halved_by_rows(shape):
    return (shape[1] // 2) % 16 == 0


def _half_of(ref, pc, block=None):
    lead = slice(None) if block is None else block
    if _halved_by_rows(ref.shape):
        h = ref.shape[1] // 2
        return ref.at[lead, pl.ds(pl.multiple_of(pc * h, 16), h), :]
    h = ref.shape[2] // 2
    return ref.at[lead, :, pl.ds(pl.multiple_of(pc * h, 128), h)]


def _half_shape(shape):
    return (shape[0], shape[1] // 2, shape[2]) if _halved_by_rows(shape) else (shape[0], shape[1], shape[2] // 2)


def _half_rows(ref, chip_id, pc):
    return _half_of(ref, pc, chip_id)


def _gather_blocks(lands, name, after=()):
    n = len(lands)
    n_in = n + len(after)

    def body(*refs):
        outs = refs[n_in:n_in + n]
        send_sems, recv_sems = refs[n_in + n:]
        x, y, c, chips = _place()
        me_chip = 2 * x + y
        sibling = (x, y, 1 - c)

        def copy(k, j, chip_id, pc, to):
            return pltpu.make_async_remote_copy(
                src_ref=_half_rows(outs[k], chip_id, pc), dst_ref=_half_rows(outs[k], chip_id, pc),
                send_sem=send_sems.at[k, j], recv_sem=recv_sems.at[k, j], device_id=to, device_id_type=MESH)

        started = []
        for k in range(n):
            for j, chip in enumerate(chips):
                cp = copy(k, j, me_chip, c, (*chip, c))
                cp.start()
                started.append(cp)
        for k in range(n):
            for j, (px, py) in enumerate(chips):
                copy(k, j, 2 * px + py, c, sibling).wait_recv()
                cp = copy(k, 3 + j, 2 * px + py, c, sibling)
                cp.start()
                started.append(cp)
        for k in range(n):
            for j, (px, py) in enumerate(chips):
                copy(k, 3 + j, 2 * px + py, 1 - c, sibling).wait_recv()
        for cp in started:
            cp.wait_send()

    return pl.pallas_call(
        body, name=name,
        out_shape=[jax.ShapeDtypeStruct(l.shape, l.dtype) for l in lands],
        in_specs=[ANY_SPEC] * n_in, out_specs=[ANY_SPEC] * n,
        input_output_aliases={i: i for i in range(n)},
        scratch_shapes=[pltpu.SemaphoreType.DMA((n, 6)), pltpu.SemaphoreType.DMA((n, 6))],
    )(*lands, *after)


def _hbm(a):
    return pltpu.with_memory_space_constraint(a, pltpu.HBM)


def _blocks_start(lands, name, after=()):
    n = len(lands)
    n_sem = 3 * n
    first = n + len(after)

    def body(*refs):
        lnd = refs[:n]
        send_sems, recv_sems = refs[first:first + n_sem], refs[first + n_sem:first + 2 * n_sem]
        token = refs[-1]
        x, y, c, chips = _place()
        me_chip = 2 * x + y
        for k in range(n):
            for j, chip in enumerate(chips):
                pltpu.make_async_remote_copy(
                    src_ref=_half_rows(lnd[k], me_chip, c), dst_ref=_half_rows(lnd[k], me_chip, c),
                    send_sem=send_sems[3 * k + j], recv_sem=recv_sems[3 * k + j],
                    device_id=(*chip, c), device_id_type=MESH).start()
        token[...] = jnp.zeros_like(token)

    out = pl.pallas_call(
        body, name=name,
        out_shape=(*[pltpu.SemaphoreType.DMA(())] * (2 * n_sem),
                   *[pltpu.HBM(l.shape, l.dtype) for l in lands],
                   jax.ShapeDtypeStruct((8, 128), F32)),
        in_specs=[HBM_SPEC] * n + [ANY_SPEC] * len(after),
        out_specs=(*[SEM_SPEC] * (2 * n_sem), *[HBM_SPEC] * n, VMEM_SPEC),
        input_output_aliases={i: 2 * n_sem + i for i in range(n)},
        compiler_params=pltpu.CompilerParams(has_side_effects=EFFECT),
    )(*[_hbm(l) for l in lands], *after)
    return list(out[:2 * n_sem]), list(out[2 * n_sem:2 * n_sem + n]), out[-1]


def _blocks_wait(sems, lands, after, name):
    n = len(lands)
    n_sem = 3 * n

    def body(*refs):
        lnd = refs[:n]
        s_sems, r_sems = refs[n:n + n_sem], refs[n + n_sem:n + 2 * n_sem]
        x, y, c, chips = _place()
        me_chip = 2 * x + y
        for k in range(n):
            for j, (px, py) in enumerate(chips):
                cp = pltpu.make_async_remote_copy(
                    src_ref=_half_rows(lnd[k], me_chip, c), dst_ref=_half_rows(lnd[k], 2 * px + py, c),
                    send_sem=s_sems[3 * k + j], recv_sem=r_sems[3 * k + j],
                    device_id=(px, py, c), device_id_type=MESH)
                cp.wait_send()
                cp.wait_recv()

    out = pl.pallas_call(
        body, name=name,
        out_shape=tuple(pltpu.HBM(l.shape, l.dtype) for l in lands),
        in_specs=[HBM_SPEC] * n + [SEM_SPEC] * (2 * n_sem) + [ANY_SPEC] * len(after),
        out_specs=[HBM_SPEC] * n,
        input_output_aliases={i: i for i in range(n)},
        compiler_params=pltpu.CompilerParams(has_side_effects=EFFECT),
    )(*lands, *sems, *after)
    return list(out)


def _blocks_finish(lands, name):
    n = len(lands)

    def body(*refs):
        lnd = refs[n:2 * n]
        send_sems, recv_sems = refs[2 * n:]
        x, y, c, chips = _place()
        sibling = (x, y, 1 - c)

        def copy(k, j, chip_id, pc):
            return pltpu.make_async_remote_copy(
                src_ref=_half_rows(lnd[k], chip_id, pc), dst_ref=_half_rows(lnd[k], chip_id, pc),
                send_sem=send_sems.at[k, j], recv_sem=recv_sems.at[k, j], device_id=sibling, device_id_type=MESH)

        started = []
        for k in range(n):
            for j, (px, py) in enumerate(chips):
                cp = copy(k, j, 2 * px + py, c)
                cp.start()
                started.append(cp)
        for k in range(n):
            for j, (px, py) in enumerate(chips):
                copy(k, j, 2 * px + py, 1 - c).wait_recv()
        for cp in started:
            cp.wait_send()

    out = pl.pallas_call(
        body, name=name,
        out_shape=[jax.ShapeDtypeStruct(l.shape, l.dtype) for l in lands],
        in_specs=[ANY_SPEC] * n, out_specs=[ANY_SPEC] * n,
        input_output_aliases={i: i for i in range(n)},
        scratch_shapes=[pltpu.SemaphoreType.DMA((n, 3)), pltpu.SemaphoreType.DMA((n, 3))],
    )(*lands)
    return list(out)


def _gather_start(shards, name):
    n = len(shards)
    n_sem = 3 * n

    def body(*refs):
        ins, lands = refs[:n], refs[n:2 * n]
        send_sems, recv_sems = refs[2 * n:2 * n + n_sem], refs[2 * n + n_sem:2 * n + 2 * n_sem]
        token = refs[-1]
        x, y, c, chips = _place()
        me_chip = 2 * x + y
        for k in range(n):
            h = shards[k].shape[0] // 2
            rows = pl.ds(pl.multiple_of(c * h, 8), h)
            for j, chip in enumerate(chips):
                pltpu.make_async_remote_copy(
                    src_ref=ins[k].at[rows, :], dst_ref=lands[k].at[me_chip, rows, :],
                    send_sem=send_sems[3 * k + j], recv_sem=recv_sems[3 * k + j],
                    device_id=(*chip, c), device_id_type=MESH).start()
        token[...] = jnp.zeros_like(token)

    lands = [_hbm(lax.empty((N_CHIP,) + s.shape, s.dtype)) for s in shards]
    out = pl.pallas_call(
        body, name=name,
        out_shape=(*[pltpu.SemaphoreType.DMA(())] * (2 * n_sem),
                   *[pltpu.HBM(s.shape, s.dtype) for s in shards],
                   *[pltpu.HBM(l.shape, l.dtype) for l in lands],
                   jax.ShapeDtypeStruct((8, 128), F32)),
        in_specs=[HBM_SPEC] * (2 * n),
        out_specs=(*[SEM_SPEC] * (2 * n_sem), *[HBM_SPEC] * (2 * n), VMEM_SPEC),
        input_output_aliases={i: 2 * n_sem + i for i in range(2 * n)},
        compiler_params=pltpu.CompilerParams(has_side_effects=EFFECT),
    )(*[_hbm(s) for s in shards], *lands)
    sems = list(out[:2 * n_sem])
    return sems, list(out[2 * n_sem:2 * n_sem + n]), list(out[2 * n_sem + n:2 * n_sem + 2 * n]), out[-1]


def _gather_wait(sems, shards, lands, after, name):
    n = len(shards)
    n_sem = 3 * n

    def body(*refs):
        ins, lnd = refs[:n], refs[n:2 * n]
        s_sems, r_sems = refs[2 * n:2 * n + n_sem], refs[2 * n + n_sem:2 * n + 2 * n_sem]
        x, y, c, chips = _place()
        for k in range(n):
            h = shards[k].shape[0] // 2
            rows = pl.ds(pl.multiple_of(c * h, 8), h)
            for j, (px, py) in enumerate(chips):
                cp = pltpu.make_async_remote_copy(
                    src_ref=ins[k].at[rows, :], dst_ref=lnd[k].at[2 * px + py, rows, :],
                    send_sem=s_sems[3 * k + j], recv_sem=r_sems[3 * k + j],
                    device_id=(px, py, c), device_id_type=MESH)
                cp.wait_send()
                cp.wait_recv()

    out = pl.pallas_call(
        body, name=name,
        out_shape=(*[pltpu.HBM(s.shape, s.dtype) for s in shards], *[pltpu.HBM(l.shape, l.dtype) for l in lands]),
        in_specs=[HBM_SPEC] * (2 * n) + [SEM_SPEC] * (2 * n_sem) + [ANY_SPEC],
        out_specs=[HBM_SPEC] * (2 * n),
        input_output_aliases={i: i for i in range(2 * n)},
        compiler_params=pltpu.CompilerParams(has_side_effects=EFFECT),
    )(*shards, *lands, *sems, after)
    return list(out[:n]), list(out[n:])


def _gather_finish(shards, lands, name):
    n = len(shards)

    def body(*refs):
        ins, lnd = refs[:n], refs[2 * n:3 * n]
        send_sems, recv_sems, local_sems = refs[3 * n:]
        x, y, c, chips = _place()
        me_chip = 2 * x + y
        sibling = (x, y, 1 - c)

        def half(k, chip_id, pc):
            h = shards[k].shape[0] // 2
            return lnd[k].at[chip_id, pl.ds(pl.multiple_of(pc * h, 8), h), :]

        def copy(k, j, chip_id, pc):
            return pltpu.make_async_remote_copy(
                src_ref=half(k, chip_id, pc), dst_ref=half(k, chip_id, pc),
                send_sem=send_sems.at[k, j], recv_sem=recv_sems.at[k, j], device_id=sibling, device_id_type=MESH)

        locals_, started = [], []
        for k in range(n):
            cp = pltpu.make_async_copy(ins[k], lnd[k].at[me_chip], local_sems.at[k])
            cp.start()
            locals_.append(cp)
            for j, (px, py) in enumerate(chips):
                cp = copy(k, j, 2 * px + py, c)
                cp.start()
                started.append(cp)
        for k in range(n):
            for j, (px, py) in enumerate(chips):
                copy(k, j, 2 * px + py, 1 - c).wait_recv()
        for cp in started:
            cp.wait_send()
        for cp in locals_:
            cp.wait()

    out = pl.pallas_call(
        body, name=name,
        out_shape=[jax.ShapeDtypeStruct(l.shape, l.dtype) for l in lands],
        in_specs=[ANY_SPEC] * (2 * n), out_specs=[ANY_SPEC] * n,
        input_output_aliases={n + i: i for i in range(n)},
        scratch_shapes=[pltpu.SemaphoreType.DMA((n, 3)), pltpu.SemaphoreType.DMA((n, 3)),
                        pltpu.SemaphoreType.DMA((n,))],
    )(*shards, *lands)
    return list(out)


def _send_other_half(arrs, name):
    n = len(arrs)

    def body(*refs):
        ins, outs = refs[:n], refs[n:2 * n]
        send_sems, recv_sems = refs[2 * n:]
        x, y, c, _ = _place()
        cps = []
        for k in range(n):
            cp = pltpu.make_async_remote_copy(
                src_ref=_half_of(ins[k], 1 - c), dst_ref=outs[k],
                send_sem=send_sems.at[k], recv_sem=recv_sems.at[k], device_id=(x, y, 1 - c), device_id_type=MESH)
            cp.start()
            cps.append(cp)
        for cp in cps:
            cp.wait()

    return pl.pallas_call(
        body, name=name,
        out_shape=[jax.ShapeDtypeStruct(_half_shape(a.shape), a.dtype) for a in arrs],
        in_specs=[ANY_SPEC] * n, out_specs=[ANY_SPEC] * n,
        scratch_shapes=[pltpu.SemaphoreType.DMA((n,)), pltpu.SemaphoreType.DMA((n,))],
    )(*arrs)


def _blocks_to_owner(arrs, name):
    n = len(arrs)

    def body(*refs):
        ins, outs = refs[:n], refs[n:2 * n]
        send_sems, recv_sems, local_sems = refs[2 * n:]
        x, y, c, chips = _place()
        me_chip = 2 * x + y
        locals_, started = [], []
        for k in range(n):
            cp = pltpu.make_async_copy(ins[k].at[me_chip], outs[k].at[me_chip], local_sems.at[k])
            cp.start()
            locals_.append(cp)

        def copy(k, j, src_block, dst_slot, to):
            return pltpu.make_async_remote_copy(
                src_ref=ins[k].at[src_block], dst_ref=outs[k].at[dst_slot],
                send_sem=send_sems.at[k, j], recv_sem=recv_sems.at[k, j], device_id=to, device_id_type=MESH)

        for k in range(n):
            for j, (px, py) in enumerate(chips):
                cp = copy(k, j, 2 * px + py, me_chip, (px, py, c))
                cp.start()
                started.append(cp)
        for k in range(n):
            for j, (px, py) in enumerate(chips):
                copy(k, j, me_chip, 2 * px + py, (px, py, c)).wait_recv()
        for cp in started:
            cp.wait_send()
        for cp in locals_:
            cp.wait()

    return pl.pallas_call(
        body, name=name,
        out_shape=[jax.ShapeDtypeStruct(a.shape, a.dtype) for a in arrs],
        in_specs=[ANY_SPEC] * n, out_specs=[ANY_SPEC] * n,
        scratch_shapes=[pltpu.SemaphoreType.DMA((n, 3)), pltpu.SemaphoreType.DMA((n, 3)),
                        pltpu.SemaphoreType.DMA((n,))],
    )(*arrs)


def _scatter_blocks(arrs, name):
    n = len(arrs)

    def body(*refs):
        ins, outs = refs[:n], refs[n:2 * n]
        send_sems, recv_sems = refs[2 * n:]
        x, y, c, chips = _place()
        me_chip = 2 * x + y

        def copy(k, j, src_block, dst_slot, to):
            return pltpu.make_async_remote_copy(
                src_ref=ins[k].at[src_block], dst_ref=outs[k].at[dst_slot],
                send_sem=send_sems.at[k, j], recv_sem=recv_sems.at[k, j], device_id=to, device_id_type=MESH)

        started = []
        for k in range(n):
            for j, (px, py) in enumerate(chips):
                cp = copy(k, j, 2 * px + py, me_chip, (px, py, c))
                cp.start()
                started.append(cp)
        for k in range(n):
            for j, (px, py) in enumerate(chips):
                copy(k, j, me_chip, 2 * px + py, (px, py, c)).wait_recv()
        for cp in started:
            cp.wait_send()

    return pl.pallas_call(
        body, name=name,
        out_shape=[jax.ShapeDtypeStruct(a.shape, a.dtype) for a in arrs],
        in_specs=[ANY_SPEC] * n, out_specs=[ANY_SPEC] * n,
        scratch_shapes=[pltpu.SemaphoreType.DMA((n, 3)), pltpu.SemaphoreType.DMA((n, 3))],
    )(*arrs)


def _scatter_start(arrs, name, after=()):
    n = len(arrs)
    n_sem = 3 * n
    first = 2 * n + len(after)

    def body(*refs):
        ins, lnd = refs[:n], refs[n:2 * n]
        send_sems, recv_sems = refs[first:first + n_sem], refs[first + n_sem:first + 2 * n_sem]
        token = refs[-1]
        x, y, c, chips = _place()
        me_chip = 2 * x + y
        for k in range(n):
            for j, (px, py) in enumerate(chips):
                pltpu.make_async_remote_copy(
                    src_ref=ins[k].at[2 * px + py], dst_ref=lnd[k].at[me_chip],
                    send_sem=send_sems[3 * k + j], recv_sem=recv_sems[3 * k + j],
                    device_id=(px, py, c), device_id_type=MESH).start()
        token[...] = jnp.zeros_like(token)

    lands = [_hbm(lax.empty(a.shape, a.dtype)) for a in arrs]
    out = pl.pallas_call(
        body, name=name,
        out_shape=(*[pltpu.SemaphoreType.DMA(())] * (2 * n_sem),
                   *[pltpu.HBM(a.shape, a.dtype) for a in arrs], *[pltpu.HBM(a.shape, a.dtype) for a in arrs],
                   jax.ShapeDtypeStruct((8, 128), F32)),
        in_specs=[HBM_SPEC] * (2 * n) + [ANY_SPEC] * len(after),
        out_specs=(*[SEM_SPEC] * (2 * n_sem), *[HBM_SPEC] * (2 * n), VMEM_SPEC),
        input_output_aliases={i: 2 * n_sem + i for i in range(2 * n)},
        compiler_params=pltpu.CompilerParams(has_side_effects=EFFECT),
    )(*[_hbm(a) for a in arrs], *lands, *after)
    base = 2 * n_sem
    return list(out[:base]), list(out[base:base + n]), list(out[base + n:base + 2 * n]), out[-1]


def _scatter_wait(sems, arrs, lands, after, name):
    n = len(arrs)
    n_sem = 3 * n

    def body(*refs):
        ins, lnd = refs[:n], refs[n:2 * n]
        s_sems, r_sems = refs[2 * n:2 * n + n_sem], refs[2 * n + n_sem:2 * n + 2 * n_sem]
        x, y, c, chips = _place()
        for k in range(n):
            for j, (px, py) in enumerate(chips):
                cp = pltpu.make_async_remote_copy(
                    src_ref=ins[k].at[2 * px + py], dst_ref=lnd[k].at[2 * px + py],
                    send_sem=s_sems[3 * k + j], recv_sem=r_sems[3 * k + j],
                    device_id=(px, py, c), device_id_type=MESH)
                cp.wait_send()
                cp.wait_recv()

    out = pl.pallas_call(
        body, name=name,
        out_shape=tuple(pltpu.HBM(a.shape, a.dtype) for a in list(arrs) + list(lands)),
        in_specs=[HBM_SPEC] * (2 * n) + [SEM_SPEC] * (2 * n_sem) + [ANY_SPEC] * len(after),
        out_specs=[HBM_SPEC] * (2 * n),
        input_output_aliases={i: i for i in range(2 * n)},
        compiler_params=pltpu.CompilerParams(has_side_effects=EFFECT),
    )(*arrs, *lands, *sems, *after)
    return list(out[:n]), list(out[n:])


def _sum_owner(chip_arr, pairs, got, name):
    nb, h, cols = got.shape
    tr = _row_tile(h, 16, 256)

    def body(chip_ref, own_ref, a_ref, b_ref, c_ref, o_ref):
        o_ref[...] = ((own_ref[0].astype(F32) + a_ref[0].astype(F32)) + b_ref[0].astype(F32)) + c_ref[0].astype(F32)

    def slot(off):
        return pl.BlockSpec((1, tr, cols), lambda i, chip_ref: ((chip_ref[0] + off) % N_CHIP, i, 0))

    return pl.pallas_call(
        body, name=name,
        grid_spec=pltpu.PrefetchScalarGridSpec(
            num_scalar_prefetch=1, grid=(h // tr,),
            in_specs=[slot(0), slot(1), slot(2), slot(3)],
            out_specs=pl.BlockSpec((tr, cols), lambda i, chip_ref: (i, 0))),
        out_shape=jax.ShapeDtypeStruct((h, cols), F32),
        compiler_params=_cparams(dimension_semantics=("parallel",)),
    )(chip_arr, pairs, got, got, got)


def _swap_with_sibling(arrs, name):
    n = len(arrs)

    def body(*refs):
        ins, outs = refs[:n], refs[n:2 * n]
        send_sems, recv_sems = refs[2 * n:]
        x, y, c, _ = _place()
        cps = []
        for k in range(n):
            cp = pltpu.make_async_remote_copy(
                src_ref=ins[k], dst_ref=outs[k], send_sem=send_sems.at[k], recv_sem=recv_sems.at[k],
                device_id=(x, y, 1 - c), device_id_type=MESH)
            cp.start()
            cps.append(cp)
        for cp in cps:
            cp.wait()

    return pl.pallas_call(
        body, name=name,
        out_shape=[jax.ShapeDtypeStruct(a.shape, a.dtype) for a in arrs],
        in_specs=[ANY_SPEC] * n, out_specs=[ANY_SPEC] * n,
        scratch_shapes=[pltpu.SemaphoreType.DMA((n,)), pltpu.SemaphoreType.DMA((n,))],
    )(*arrs)


def _row_tile(h, mult=8, cap=128):
    for t in range(cap - cap % mult, mult - 1, -mult):
        if h % t == 0:
            return t
    if mult > 8:
        return _row_tile(h, 8, cap)
    raise ValueError(h)


def _cast_bf16(a, name):
    rows, cols = a.shape
    tr = _row_tile(rows, 16, 256)

    def body(a_ref, o_ref):
        o_ref[...] = a_ref[...].astype(BF16)

    return pl.pallas_call(
        body, name=name, grid=(rows // tr,),
        out_shape=jax.ShapeDtypeStruct(a.shape, BF16),
        in_specs=[pl.BlockSpec((tr, cols), lambda i: (i, 0))],
        out_specs=pl.BlockSpec((tr, cols), lambda i: (i, 0)),
        compiler_params=_cparams(dimension_semantics=("parallel",)),
    )(a)


def _pair_sum(c_arr, full, recv, name):
    nb, rows, cols = full.shape

    def body(c_ref, f_ref, r_ref, o_ref):
        o_ref[...] = (f_ref[...] + r_ref[...]).astype(BF16)

    if _halved_by_rows(full.shape):
        h = rows // 2
        tr = _row_tile(h, 16, 256)
        steps = h // tr
        own = pl.BlockSpec((1, tr, cols), lambda b, i, c_ref: (b, c_ref[0] * steps + i, 0))
        half = pl.BlockSpec((1, tr, cols), lambda b, i, c_ref: (b, i, 0))
    else:
        steps = 1
        own = pl.BlockSpec((1, rows, cols // 2), lambda b, i, c_ref: (b, 0, c_ref[0]))
        half = pl.BlockSpec((1, rows, cols // 2), lambda b, i, c_ref: (b, 0, 0))
    return pl.pallas_call(
        body, name=name,
        grid_spec=pltpu.PrefetchScalarGridSpec(
            num_scalar_prefetch=1, grid=(nb, steps), in_specs=[own, half], out_specs=half),
        out_shape=jax.ShapeDtypeStruct(_half_shape(full.shape), BF16),
        compiler_params=_cparams(dimension_semantics=("parallel", "parallel")),
    )(c_arr, full, recv)


def _sum_chips(got, name):
    nb, h, cols = got.shape
    tr = _row_tile(h, 16, 256)

    def body(g_ref, o_ref):
        g = g_ref[...].astype(F32)
        o_ref[...] = ((g[0] + g[1]) + g[2]) + g[3]

    return pl.pallas_call(
        body, name=name, grid=(h // tr,),
        out_shape=jax.ShapeDtypeStruct((h, cols), F32),
        in_specs=[pl.BlockSpec((nb, tr, cols), lambda i: (0, i, 0))],
        out_specs=pl.BlockSpec((tr, cols), lambda i: (i, 0)),
        compiler_params=_cparams(dimension_semantics=("parallel",)),
    )(got)


def _adam_math(g, w, m, v):
    m1 = ADAM_B1 * m + (1.0 - ADAM_B1) * g
    v1 = ADAM_B2 * v + (1.0 - ADAM_B2) * (g * g)
    m_hat = m1 / (1.0 - ADAM_B1 ** ADAM_STEP)
    v_hat = v1 / (1.0 - ADAM_B2 ** ADAM_STEP)
    delta = -ADAM_LR * (m_hat / (jnp.sqrt(v_hat) + ADAM_EPS) + ADAM_WD * w)
    return delta, m1, v1


def _adamw_halves(c_arr, own, other, w, m, v, name):
    rows, cols = w.shape

    def body(c_ref, own_ref, oth_ref, w_ref, m_ref, v_ref, g_out, d_out, m_out, v_out):
        g = jnp.where(pl.program_id(0) == c_ref[0], own_ref[...], oth_ref[...])
        d, m1, v1 = _adam_math(g, w_ref[...], m_ref[...], v_ref[...])
        g_out[...] = g
        d_out[...] = d
        m_out[...] = m1
        v_out[...] = v1

    if own.shape[1] == cols:
        h = rows // 2
        tr = _row_tile(h)
        steps = h // tr
        half_spec = pl.BlockSpec((tr, cols), lambda p, i, c_ref: (i, 0))
        full_spec = pl.BlockSpec((tr, cols), lambda p, i, c_ref: (p * steps + i, 0))
    else:
        tr = _row_tile(rows)
        steps = rows // tr
        half_spec = pl.BlockSpec((tr, cols // 2), lambda p, i, c_ref: (i, 0))
        full_spec = pl.BlockSpec((tr, cols // 2), lambda p, i, c_ref: (i, p))
    return pl.pallas_call(
        body, name=name,
        grid_spec=pltpu.PrefetchScalarGridSpec(
            num_scalar_prefetch=1, grid=(2, steps),
            in_specs=[half_spec, half_spec, full_spec, full_spec, full_spec],
            out_specs=[full_spec] * 4),
        out_shape=[jax.ShapeDtypeStruct(w.shape, F32)] * 4,
        compiler_params=_cparams(dimension_semantics=("parallel", "parallel")),
    )(c_arr, own, other, w, m, v)


def _adamw_whole(items, name):
    n = len(items)

    def body(*refs):
        ins, outs = refs[:4 * n], refs[4 * n:]
        for k in range(n):
            g, w, m, v = (r[...] for r in ins[4 * k:4 * k + 4])
            d, m1, v1 = _adam_math(g, w, m, v)
            outs[3 * k][...] = d
            outs[3 * k + 1][...] = m1
            outs[3 * k + 2][...] = v1

    flat = [a for it in items for a in it]
    shapes = [jax.ShapeDtypeStruct(it[1].shape, F32) for it in items for _ in range(3)]
    out = pl.pallas_call(
        body, name=name, out_shape=shapes,
        in_specs=[VMEM_SPEC] * (4 * n), out_specs=[VMEM_SPEC] * (3 * n),
        compiler_params=_cparams(),
    )(*flat)
    return [tuple(out[3 * k:3 * k + 3]) for k in range(n)]


def _adamw_tiled(g, w, m, v, name):
    rows, cols = w.shape
    tr = _row_tile(rows)

    def body(g_ref, w_ref, m_ref, v_ref, d_out, m_out, v_out):
        d, m1, v1 = _adam_math(g_ref[...], w_ref[...], m_ref[...], v_ref[...])
        d_out[...] = d
        m_out[...] = m1
        v_out[...] = v1

    spec = pl.BlockSpec((tr, cols), lambda i: (i, 0))
    return pl.pallas_call(
        body, name=name, grid=(rows // tr,),
        out_shape=[jax.ShapeDtypeStruct(w.shape, F32)] * 3,
        in_specs=[spec] * 4, out_specs=[spec] * 3,
        compiler_params=_cparams(dimension_semantics=("parallel",)),
    )(g, w, m, v)


def _mod_forward(cond, w_mod, b_mod_cols, name):
    def body(c_ref, w_ref, b_ref, o_ref):
        o_ref[...] = _dot(_silu(c_ref[...]), w_ref[...]) + b_ref[...]

    return pl.pallas_call(
        body, name=name, out_shape=jax.ShapeDtypeStruct((cond.shape[0], w_mod.shape[1]), F32),
        in_specs=[VMEM_SPEC] * 3, out_specs=VMEM_SPEC, compiler_params=_cparams(),
    )(cond, w_mod, b_mod_cols)


def _mod_backward(cond, w_mod, dmod_cols, name):
    def body(c_ref, w_ref, d_ref, gw_ref, gc_ref):
        s = _silu(c_ref[...])
        d = d_ref[...]
        gw_ref[...] = _dot_tn(s, d)
        gc_ref[...] = _dot_nt(d[8:16, :], w_ref[...])

    return pl.pallas_call(
        body, name=name,
        out_shape=[jax.ShapeDtypeStruct(w_mod.shape, F32), jax.ShapeDtypeStruct((8, w_mod.shape[0]), F32)],
        in_specs=[VMEM_SPEC] * 3, out_specs=[VMEM_SPEC] * 2, compiler_params=_cparams(),
    )(cond, w_mod, dmod_cols)


def _col_chunks(width, step=512):
    return [(s, min(step, width - s)) for s in range(0, width, step)]


def _w_in_row(p_off):
    if p_off < 9 * HW:
        return p_off
    return 9 * HW if p_off == OFF_LR else p_off + 2 * RANK


def _in_projection(ctx0, x0, modc, modx, pre1, w_t, n_ctx_tiles, name):
    d = x0.shape[1]
    rows = ctx0.shape[0] + x0.shape[0]
    width = P_WIDTH

    def body(ctx_ref, x_ref, modc_ref, modx_ref, pre_ref, w_ref, h_ref, p_ref):
        is_ctx = pl.program_id(0) < n_ctx_tiles
        n, _ = _rms(jnp.where(is_ctx, ctx_ref[...], x_ref[...]))
        shift = jnp.where(is_ctx, modc_ref[0:1, :], modx_ref[0:1, :])
        scale = jnp.where(is_ctx, modc_ref[1:2, :], modx_ref[1:2, :])
        h = (n * pre_ref[...] * (1.0 + scale) + shift).astype(BF16)
        h_ref[...] = h
        for s, w in _col_chunks(width):
            p_ref[:, s:s + w] = _dot_nt(h, w_ref[_w_in_row(s):_w_in_row(s) + w, :])

    row = lambda i: (i, 0)
    fixed = lambda i: (0, 0)
    return pl.pallas_call(
        body, name=name, grid=(rows // TM,),
        out_shape=[jax.ShapeDtypeStruct((rows, d), BF16), jax.ShapeDtypeStruct((rows, width), F32)],
        in_specs=[pl.BlockSpec((TM, d), lambda i: (jnp.minimum(i, n_ctx_tiles - 1), 0)),
                  pl.BlockSpec((TM, d), lambda i: (jnp.maximum(i - n_ctx_tiles, 0), 0)),
                  pl.BlockSpec((8, d), fixed), pl.BlockSpec((8, d), fixed), pl.BlockSpec((1, d), fixed), VMEM_SPEC],
        out_specs=[pl.BlockSpec((TM, d), row), pl.BlockSpec((TM, width), row)],
        compiler_params=_cparams(dimension_semantics=("parallel",)),
    )(ctx0, x0, modc, modx, pre1, w_t)


C_HQ, C_HI, C_HF_FW, C_HF_BW, C_HGATE, C_GQ, C_GK, C_GV, C_GGATE = range(9)
OFF_GATE_HG = 9 * HW
OFF_LR = 13 * HW
P_WIDTH = OFF_LR + 128


def _head_norm_fwd(o, w):
    outs, ns, rs = [], [], []
    for h in range(NH):
        n, r = _rms(o[:, h * HD:(h + 1) * HD])
        ns.append(n)
        rs.append(r)
        outs.append(n * w)
    return jnp.concatenate(outs, axis=1), ns, rs


def _mixer_tail(z, o_hg, o_gla, p_hgate, p_ggate, p_gate_hg, p_gate_gla, hg_on, gla_on, wbh, wbg, wout):
    on_hg, n_hg, r_hg = _head_norm_fwd(o_hg, hg_on)
    on_gla, n_gla, r_gla = _head_norm_fwd(o_gla, gla_on)
    og_hg = (on_hg * _silu(p_hgate)).astype(BF16)
    og_gla = (on_gla * _silu(p_ggate)).astype(BF16)
    b_hg = jnp.dot(og_hg, wbh, preferred_element_type=F32)
    b_gla = jnp.dot(og_gla, wbg, preferred_element_type=F32)
    s_hg = _sigmoid(p_gate_hg)
    s_gla = _sigmoid(p_gate_gla)
    merged = (s_hg * b_hg + s_gla * b_gla).astype(BF16)
    y1 = jnp.dot(merged, wout, preferred_element_type=F32)
    return dict(on_hg=on_hg, n_hg=n_hg, r_hg=r_hg, on_gla=on_gla, n_gla=n_gla, r_gla=r_gla, og_hg=og_hg,
                og_gla=og_gla, b_hg=b_hg, b_gla=b_gla, s_hg=s_hg, s_gla=s_gla, merged=merged, y1=y1)


def _mixer_ffn(x_lat, p, o_list, modx, norms, onorms, w_br_hg, w_br_gla, w_out, w_gate, w_up, w_down, target,
               n_ctx_tiles, name):
    rows, d = x_lat.shape
    dff = w_gate.shape[0]
    inv_d = 1.0 / d

    def body(x_ref, ofw_hg, obw_hg, ofw_gla, obw_gla, p_hgate, p_ggate, p_ghg_a, p_ghg_b, p_ggla_a, p_ggla_b,
             modx_ref, norm_ref, on_ref, wbh_ref, wbg_ref, wout_ref, wg_ref, wu_ref, wd_ref, t_ref,
             loss_ref, dz2_ref, y1_ref, mrg_ref, oghg_ref, oggla_ref, h2_ref, a_ref, du_ref, dv_ref, dy2_ref,
             stat_ref):
        i = pl.program_id(0)
        post1, pre2, post2 = norm_ref[1:2, :], norm_ref[2:3, :], norm_ref[3:4, :]
        gate1, shift2, scale2, gate2 = modx_ref[2:3, :], modx_ref[3:4, :], modx_ref[4:5, :], modx_ref[5:6, :]
        p_gate_hg = jnp.concatenate([p_ghg_a[...], p_ghg_b[...]], axis=1)
        p_gate_gla = jnp.concatenate([p_ggla_a[...], p_ggla_b[...]], axis=1)
        t = _mixer_tail(x_ref[...], ofw_hg[...] + obw_hg[...], ofw_gla[...] + obw_gla[...], p_hgate[...],
                        p_ggate[...], p_gate_hg, p_gate_gla, on_ref[0:1, 0:HD], on_ref[1:2, 0:HD],
                        wbh_ref[...], wbg_ref[...], wout_ref[...])
        y1_ref[...] = t["y1"]
        mrg_ref[...] = t["merged"]
        oghg_ref[...] = t["og_hg"]
        oggla_ref[...] = t["og_gla"]
        n1, _ = _rms(t["y1"])
        z2 = x_ref[...] + n1 * post1 * gate1
        n2, r2 = _rms(z2)
        nw2 = n2 * pre2
        h2 = (nw2 * (1.0 + scale2) + shift2).astype(BF16)
        h2_ref[...] = h2
        u = _dot_nt(h2, wg_ref[...])
        v = _dot_nt(h2, wu_ref[...])
        su = _silu(u)
        a = (su * v).astype(BF16)
        a_ref[...] = a
        y2 = jnp.dot(a, wd_ref[...], preferred_element_type=F32)
        n3, r3 = _rms(y2)
        z3 = z2 + n3 * post2 * gate2
        err = z3 - t_ref[...]
        part = 0.5 * inv_d * jnp.sum(err * err)
        dz3 = err * inv_d
        dgate2 = _colsum(dz3 * n3 * post2)
        tt = dz3 * gate2
        dpost2 = _colsum(tt * n3)
        dy2 = _rms_bwd(tt * post2, n3, r3).astype(BF16)
        dy2_ref[...] = dy2
        da = _dot_nt(dy2, wd_ref[...])
        du = (da * v * _dsilu(u)).astype(BF16)
        dv = (da * su).astype(BF16)
        du_ref[...] = du
        dv_ref[...] = dv
        dh2 = (jnp.dot(du, wg_ref[...], preferred_element_type=F32)
               + jnp.dot(dv, wu_ref[...], preferred_element_type=F32))
        dshift2 = _colsum(dh2)
        dscale2 = _colsum(dh2 * nw2)
        dnw2 = dh2 * (1.0 + scale2)
        dpre2 = _colsum(dnw2 * n2)
        dz2_ref[...] = dz3 + _rms_bwd(dnw2 * pre2, n2, r2)

        @pl.when(i == 0)
        def _():
            stat_ref[...] = jnp.zeros_like(stat_ref)
            loss_ref[...] = jnp.zeros_like(loss_ref)

        for r, val in enumerate((dshift2, dscale2, dgate2, dpre2, dpost2)):
            stat_ref[r:r + 1, :] += val
        loss_ref[...] += part
        stat_ref[5:6, 0:128] += part

    tm = TM_FFN
    ctx_tiles = n_ctx_tiles * (TM // tm)
    lat = lambda i: (i, 0)
    full = lambda i: (i + ctx_tiles, 0)
    fixed = lambda i: (0, 0)

    def pcol(blk):
        return pl.BlockSpec((tm, HW), lambda i: (i + ctx_tiles, blk))

    in_specs = ([pl.BlockSpec((tm, d), lat)] + [pl.BlockSpec((tm, HW), full)] * 4
                + [pcol(C_HGATE), pcol(C_GGATE), pcol(9), pcol(10), pcol(11), pcol(12)]
                + [pl.BlockSpec((8, d), fixed), pl.BlockSpec((8, d), fixed), pl.BlockSpec((8, d), fixed)]
                + [VMEM_SPEC] * 6 + [pl.BlockSpec((tm, d), lat)])
    bf = lambda w: jax.ShapeDtypeStruct((rows, w), BF16)
    out_shape = [jax.ShapeDtypeStruct((8, 128), F32), jax.ShapeDtypeStruct((rows, d), F32),
                 jax.ShapeDtypeStruct((rows, d), F32), bf(d), bf(HW), bf(HW), bf(d), bf(dff), bf(dff), bf(dff), bf(d),
                 jax.ShapeDtypeStruct((8, d), F32)]
    out_specs = [pl.BlockSpec((8, 128), fixed), pl.BlockSpec((tm, d), lat), pl.BlockSpec((tm, d), lat),
                 pl.BlockSpec((tm, d), lat), pl.BlockSpec((tm, HW), lat), pl.BlockSpec((tm, HW), lat),
                 pl.BlockSpec((tm, d), lat), pl.BlockSpec((tm, dff), lat), pl.BlockSpec((tm, dff), lat),
                 pl.BlockSpec((tm, dff), lat), pl.BlockSpec((tm, d), lat), pl.BlockSpec((8, d), fixed)]
    return pl.pallas_call(
        body, name=name, grid=(rows // tm,), out_shape=out_shape, in_specs=in_specs, out_specs=out_specs,
        compiler_params=_cparams(dimension_semantics=("arbitrary",)),
    )(x_lat, *o_list, p, p, p, p, p, p, modx, norms, onorms, w_br_hg, w_br_gla, w_out, w_gate, w_up, w_down, target)


def _mixer_tail_fwd(x_lat, p, o_list, modx, norms, onorms, w_br_hg, w_br_gla, w_out, n_ctx_tiles, name):
    rows, d = x_lat.shape

    def body(x_ref, ofw_hg, obw_hg, ofw_gla, obw_gla, p_hgate, p_ggate, p_ghg_a, p_ghg_b, p_ggla_a, p_ggla_b,
             modx_ref, norm_ref, on_ref, wbh_ref, wbg_ref, wout_ref, z2_ref, y1_ref, mrg_ref, oghg_ref, oggla_ref):
        p_gate_hg = jnp.concatenate([p_ghg_a[...], p_ghg_b[...]], axis=1)
        p_gate_gla = jnp.concatenate([p_ggla_a[...], p_ggla_b[...]], axis=1)
        t = _mixer_tail(x_ref[...], ofw_hg[...] + obw_hg[...], ofw_gla[...] + obw_gla[...], p_hgate[...],
                        p_ggate[...], p_gate_hg, p_gate_gla, on_ref[0:1, 0:HD], on_ref[1:2, 0:HD],
                        wbh_ref[...], wbg_ref[...], wout_ref[...])
        y1_ref[...] = t["y1"]
        mrg_ref[...] = t["merged"]
        oghg_ref[...] = t["og_hg"]
        oggla_ref[...] = t["og_gla"]
        n1, _ = _rms(t["y1"])
        z2_ref[...] = x_ref[...] + n1 * norm_ref[1:2, :] * modx_ref[2:3, :]

    lat = lambda i: (i, 0)
    full = lambda i: (i + n_ctx_tiles, 0)
    fixed = lambda i: (0, 0)

    def pcol(blk):
        return pl.BlockSpec((TM, HW), lambda i: (i + n_ctx_tiles, blk))

    in_specs = ([pl.BlockSpec((TM, d), lat)] + [pl.BlockSpec((TM, HW), full)] * 4
                + [pcol(C_HGATE), pcol(C_GGATE), pcol(9), pcol(10), pcol(11), pcol(12)]
                + [pl.BlockSpec((8, d), fixed)] * 3 + [VMEM_SPEC] * 3)
    bf = lambda w: jax.ShapeDtypeStruct((rows, w), BF16)
    f32 = jax.ShapeDtypeStruct((rows, d), F32)
    return pl.pallas_call(
        body, name=name, grid=(rows // TM,), out_shape=[f32, f32, bf(d), bf(HW), bf(HW)], in_specs=in_specs,
        out_specs=[pl.BlockSpec((TM, d), lat)] * 3 + [pl.BlockSpec((TM, HW), lat)] * 2,
        compiler_params=_cparams(dimension_semantics=("parallel",)),
    )(x_lat, *o_list, p, p, p, p, p, p, modx, norms, onorms, w_br_hg, w_br_gla, w_out)


def _ffn_fwd_bwd(z2, modx, norms, w_gate, w_up, w_down, target, name):
    rows, d = z2.shape
    dff = w_gate.shape[0]
    inv_d = 1.0 / d

    def body(z2_ref, modx_ref, norm_ref, wg_ref, wu_ref, wd_ref, t_ref,
             loss_ref, dz2_ref, h2_ref, a_ref, du_ref, dv_ref, dy2_ref, stat_ref):
        i = pl.program_id(0)
        pre2, post2 = norm_ref[2:3, :], norm_ref[3:4, :]
        shift2, scale2, gate2 = modx_ref[3:4, :], modx_ref[4:5, :], modx_ref[5:6, :]
        z2 = z2_ref[...]
        n2, r2 = _rms(z2)
        nw2 = n2 * pre2
        h2 = (nw2 * (1.0 + scale2) + shift2).astype(BF16)
        h2_ref[...] = h2
        u = _dot_nt(h2, wg_ref[...])
        v = _dot_nt(h2, wu_ref[...])
        su = _silu(u)
        a = (su * v).astype(BF16)
        a_ref[...] = a
        y2 = jnp.dot(a, wd_ref[...], preferred_element_type=F32)
        n3, r3 = _rms(y2)
        err = z2 + n3 * post2 * gate2 - t_ref[...]
        part = 0.5 * inv_d * jnp.sum(err * err)
        dz3 = err * inv_d
        dgate2 = _colsum(dz3 * n3 * post2)
        tt = dz3 * gate2
        dpost2 = _colsum(tt * n3)
        dy2 = _rms_bwd(tt * post2, n3, r3).astype(BF16)
        dy2_ref[...] = dy2
        da = _dot_nt(dy2, wd_ref[...])
        du = (da * v * _dsilu(u)).astype(BF16)
        dv = (da * su).astype(BF16)
        du_ref[...] = du
        dv_ref[...] = dv
        dh2 = (jnp.dot(du, wg_ref[...], preferred_element_type=F32)
               + jnp.dot(dv, wu_ref[...], preferred_element_type=F32))
        dshift2 = _colsum(dh2)
        dscale2 = _colsum(dh2 * nw2)
        dnw2 = dh2 * (1.0 + scale2)
        dpre2 = _colsum(dnw2 * n2)
        dz2_ref[...] = dz3 + _rms_bwd(dnw2 * pre2, n2, r2)

        @pl.when(i == 0)
        def _():
            stat_ref[...] = jnp.zeros_like(stat_ref)
            loss_ref[...] = jnp.zeros_like(loss_ref)

        for r, val in enumerate((dshift2, dscale2, dgate2, dpre2, dpost2)):
            stat_ref[r:r + 1, :] += val
        loss_ref[...] += part
        stat_ref[5:6, 0:128] += part

    lat = lambda i: (i, 0)
    fixed = lambda i: (0, 0)
    bf = lambda w: jax.ShapeDtypeStruct((rows, w), BF16)
    return pl.pallas_call(
        body, name=name, grid=(rows // TM,),
        out_shape=[jax.ShapeDtypeStruct((8, 128), F32), jax.ShapeDtypeStruct((rows, d), F32), bf(d), bf(dff), bf(dff),
                   bf(dff), bf(d), jax.ShapeDtypeStruct((8, d), F32)],
        in_specs=[pl.BlockSpec((TM, d), lat), pl.BlockSpec((8, d), fixed), pl.BlockSpec((8, d), fixed)]
        + [VMEM_SPEC] * 3 + [pl.BlockSpec((TM, d), lat)],
        out_specs=[pl.BlockSpec((8, 128), fixed), pl.BlockSpec((TM, d), lat), pl.BlockSpec((TM, d), lat),
                   pl.BlockSpec((TM, dff), lat), pl.BlockSpec((TM, dff), lat), pl.BlockSpec((TM, dff), lat),
                   pl.BlockSpec((TM, d), lat), pl.BlockSpec((8, d), fixed)],
        compiler_params=_cparams(dimension_semantics=("arbitrary",)),
    )(z2, modx, norms, w_gate, w_up, w_down, target)


def _mixer_tail_bwd(x_lat, p, o_list, dz2, y1, modx, norms, onorms, w_br_hg, w_br_gla, w_out, n_ctx_tiles, n_tiles,
                    name):
    rows, d = x_lat.shape
    total = n_tiles * TM

    def body(x_ref, ofw_hg, obw_hg, ofw_gla, obw_gla, p_hgate, p_ggate, p_ghg_a, p_ghg_b, p_ggla_a, p_ggla_b,
             dz2_ref, y1_ref, modx_ref, norm_ref, on_ref, wbh_ref, wbg_ref, wout_ref,
             dohg_ref, dogla_ref, dhgate_ref, dggate_ref, dghg_ref, dggla_ref, dy1_ref, dbhg_ref, dbgla_ref,
             stat_ref):
        i = pl.program_id(0)

        @pl.when(i == 0)
        def _():
            stat_ref[...] = jnp.zeros_like(stat_ref)

        @pl.when(i < n_ctx_tiles)
        def _():
            for ref in (dohg_ref, dogla_ref, dhgate_ref, dggate_ref, dghg_ref, dggla_ref):
                ref[...] = jnp.zeros_like(ref)

        @pl.when(i >= n_ctx_tiles)
        def _():
            post1, gate1 = norm_ref[1:2, :], modx_ref[2:3, :]
            hg_on, gla_on = on_ref[0:1, 0:HD], on_ref[1:2, 0:HD]
            p_gate_hg = jnp.concatenate([p_ghg_a[...], p_ghg_b[...]], axis=1)
            p_gate_gla = jnp.concatenate([p_ggla_a[...], p_ggla_b[...]], axis=1)
            ph, pg = p_hgate[...], p_ggate[...]
            t = _mixer_tail(x_ref[...], ofw_hg[...] + obw_hg[...], ofw_gla[...] + obw_gla[...], ph, pg,
                            p_gate_hg, p_gate_gla, hg_on, gla_on, wbh_ref[...], wbg_ref[...], wout_ref[...])
            dz2 = dz2_ref[...]
            n1, r1 = _rms(y1_ref[...])
            dgate1 = _colsum(dz2 * n1 * post1)
            tt = dz2 * gate1
            dpost1 = _colsum(tt * n1)
            dy1 = _rms_bwd(tt * post1, n1, r1).astype(BF16)
            dy1_ref[...] = dy1
            dmerged = _dot_nt(dy1, wout_ref[...])
            dghg_ref[...] = dmerged * t["b_hg"] * t["s_hg"] * (1.0 - t["s_hg"])
            dggla_ref[...] = dmerged * t["b_gla"] * t["s_gla"] * (1.0 - t["s_gla"])
            db_hg = (dmerged * t["s_hg"]).astype(BF16)
            db_gla = (dmerged * t["s_gla"]).astype(BF16)
            dbhg_ref[...] = db_hg
            dbgla_ref[...] = db_gla
            don_acc = []
            for (db, wb, pgate, on, ns, rs, gain, gate_ref, do_ref) in (
                    (db_hg, wbh_ref, ph, t["on_hg"], t["n_hg"], t["r_hg"], hg_on, dhgate_ref, dohg_ref),
                    (db_gla, wbg_ref, pg, t["on_gla"], t["n_gla"], t["r_gla"], gla_on, dggate_ref, dogla_ref)):
                dog = _dot_nt(db, wb[...])
                gate_ref[...] = dog * on * _dsilu(pgate)
                don = dog * _silu(pgate)
                acc = jnp.zeros((1, HD), F32)
                for h in range(NH):
                    sl = slice(h * HD, (h + 1) * HD)
                    acc = acc + _colsum(don[:, sl] * ns[h])
                    do_ref[:, sl] = _rms_bwd(don[:, sl] * gain, ns[h], rs[h])
                don_acc.append(acc)
            stat_ref[0:1, :] += dgate1
            stat_ref[1:2, :] += dpost1
            stat_ref[2:3, 0:HD] += don_acc[0]
            stat_ref[2:3, HD:2 * HD] += don_acc[1]

    lat = lambda i: (jnp.maximum(i - n_ctx_tiles, 0), 0)
    full = lambda i: (i, 0)
    fixed = lambda i: (0, 0)

    def pcol(blk):
        return pl.BlockSpec((TM, HW), lambda i: (i, blk))

    in_specs = ([pl.BlockSpec((TM, d), lat)] + [pl.BlockSpec((TM, HW), full)] * 4
                + [pcol(C_HGATE), pcol(C_GGATE), pcol(9), pcol(10), pcol(11), pcol(12)]
                + [pl.BlockSpec((TM, d), lat), pl.BlockSpec((TM, d), lat)]
                + [pl.BlockSpec((8, d), fixed)] * 3 + [VMEM_SPEC] * 3)
    f = lambda w: jax.ShapeDtypeStruct((total, w), F32)
    out_shape = [f(HW), f(HW), f(HW), f(HW), f(d), f(d), jax.ShapeDtypeStruct((rows, d), BF16),
                 jax.ShapeDtypeStruct((rows, d), BF16), jax.ShapeDtypeStruct((rows, d), BF16),
                 jax.ShapeDtypeStruct((8, d), F32)]
    out_specs = ([pl.BlockSpec((TM, HW), full)] * 4 + [pl.BlockSpec((TM, d), full)] * 2
                 + [pl.BlockSpec((TM, d), lat)] * 3 + [pl.BlockSpec((8, d), fixed)])
    return pl.pallas_call(
        body, name=name, grid=(n_tiles,), out_shape=out_shape, in_specs=in_specs, out_specs=out_specs,
        compiler_params=_cparams(dimension_semantics=("arbitrary",)),
    )(x_lat, *o_list, p, p, p, p, p, p, dz2, y1, modx, norms, onorms, w_br_hg, w_br_gla, w_out)


def _in_projection_bwd(ctx0, x0, dz2, modc, modx, pre1, w_t, pieces, n_ctx_tiles, name):
    d = x0.shape[1]
    rows = ctx0.shape[0] + x0.shape[0]
    lat_rows = dz2.shape[0]
    width = P_WIDTH
    n_pieces = len(pieces)

    def body(*refs):
        ctx_ref, x_ref, dz2_ref, modc_ref, modx_ref, pre_ref, w_ref = refs[:7]
        (dhq_f, dhq_b, dhi_f, dhi_b, dhf_f, dhf_b, dhgate, dgq_f, dgq_b, dgk_f, dgk_b, dgv_f, dgv_b, dggate,
         dghg, dggla, dlr_f, dlr_b) = refs[7:7 + n_pieces]
        dp_ref, gx_ref, stat_ref = refs[7 + n_pieces:]
        i = pl.program_id(0)
        is_ctx = i < n_ctx_tiles
        z = jnp.where(is_ctx, ctx_ref[...], x_ref[...])
        sections = [
            (0, dhq_f[...] + dhq_b[...]), (HW, dhi_f[...] + dhi_b[...]), (2 * HW, dhf_f[...]), (3 * HW, dhf_b[...]),
            (4 * HW, dhgate[...]), (5 * HW, dgq_f[...] + dgq_b[...]), (6 * HW, dgk_f[...] + dgk_b[...]),
            (7 * HW, dgv_f[...] + dgv_b[...]), (8 * HW, dggate[...]),
            (9 * HW, dghg[:, 0:HW]), (10 * HW, dghg[:, HW:2 * HW]),
            (11 * HW, dggla[:, 0:HW]), (12 * HW, dggla[:, HW:2 * HW]), (OFF_LR, dlr_f[...] + dlr_b[...])]
        dh = jnp.zeros((TM, d), F32)
        for off, val in sections:
            w = val.shape[1]
            vb = val.astype(BF16)
            dp_ref[:, off:off + w] = vb
            dh = dh + jnp.dot(vb, w_ref[_w_in_row(off):_w_in_row(off) + w, :], preferred_element_type=F32)
        n, r = _rms(z)
        pre = pre_ref[...]
        scale = jnp.where(is_ctx, modc_ref[1:2, :], modx_ref[1:2, :])
        nw = n * pre
        dshift = _colsum(dh)
        dscale = _colsum(dh * nw)
        dnw = dh * (1.0 + scale)
        dpre = _colsum(dnw * n)
        gx_ref[...] = dz2_ref[...] + _rms_bwd(dnw * pre, n, r)
        zero = jnp.zeros((1, d), F32)

        @pl.when(i == 0)
        def _():
            stat_ref[...] = jnp.zeros_like(stat_ref)

        stat_ref[0:1, :] += jnp.where(is_ctx, zero, dshift)
        stat_ref[1:2, :] += jnp.where(is_ctx, zero, dscale)
        stat_ref[2:3, :] += jnp.where(is_ctx, dshift, zero)
        stat_ref[3:4, :] += jnp.where(is_ctx, dscale, zero)
        stat_ref[4:5, :] += dpre

    full = lambda i: (i, 0)
    lat = lambda i: (jnp.maximum(i - n_ctx_tiles, 0), 0)
    fixed = lambda i: (0, 0)
    piece_specs = [pl.BlockSpec((TM, a.shape[1]), full) for a in pieces]
    in_specs = [pl.BlockSpec((TM, d), lambda i: (jnp.minimum(i, n_ctx_tiles - 1), 0)), pl.BlockSpec((TM, d), lat),
                pl.BlockSpec((TM, d), lat), pl.BlockSpec((8, d), fixed),
                pl.BlockSpec((8, d), fixed), pl.BlockSpec((1, d), fixed), VMEM_SPEC] + piece_specs
    return pl.pallas_call(
        body, name=name, grid=(rows // TM,),
        out_shape=[jax.ShapeDtypeStruct((rows, width), BF16), jax.ShapeDtypeStruct((lat_rows, d), F32),
                   jax.ShapeDtypeStruct((8, d), F32)],
        in_specs=in_specs,
        out_specs=[pl.BlockSpec((TM, width), full), pl.BlockSpec((TM, d), lat), pl.BlockSpec((8, d), fixed)],
        compiler_params=_cparams(dimension_semantics=("arbitrary",)),
    )(ctx0, x0, dz2, modc, modx, pre1, w_t, *pieces)


def _transposed_lhs_matmul(x_ref, dy_ref, o_ref, xt_ref):
    @pl.when(pl.program_id(1) == 0)
    def _():
        xt_ref[...] = x_ref[...].T

    o_ref[...] = jnp.dot(xt_ref[...], dy_ref[...], preferred_element_type=F32)


def _w_in_grad(dp, h1, n_cols, name):
    rows, d = h1.shape
    n_main = OFF_LR // HW
    lr0 = _w_in_row(OFF_LR)

    def body(x_ref, xlr_ref, h_ref, o_hbm, xt_ref, acc_ref, sem):
        i = pl.program_id(0)

        def main_copy(step):
            row = jnp.where(step < 9, step * HW, step * HW + 2 * RANK)
            return pltpu.make_async_copy(acc_ref, o_hbm.at[pl.ds(pl.multiple_of(row, 8), HW), :], sem)

        lr_copy = pltpu.make_async_copy(acc_ref.at[0:2 * RANK, :], o_hbm.at[lr0:lr0 + 2 * RANK, :], sem)

        @pl.when(i < n_main)
        def _():
            xt_ref[...] = x_ref[...].T

        @pl.when(i > 0)
        def _():
            main_copy(i - 1).wait()

        @pl.when(i < n_main)
        def _():
            acc_ref[...] = jnp.dot(xt_ref[...], h_ref[...], preferred_element_type=F32)
            main_copy(i).start()

        @pl.when(i == n_main)
        def _():
            xt_ref[0:128, :] = xlr_ref[...].T
            acc_ref[0:128, :] = jnp.dot(xt_ref[0:128, :], h_ref[...], preferred_element_type=F32)
            lr_copy.start()
            lr_copy.wait()

    return pl.pallas_call(
        body, name=name, grid=(n_main + 1,),
        out_shape=jax.ShapeDtypeStruct((n_cols, d), F32),
        in_specs=[pl.BlockSpec((rows, HW), lambda i: (0, jnp.minimum(i, n_main - 1))),
                  pl.BlockSpec((rows, 128), lambda i: (0, OFF_LR // 128)),
                  pl.BlockSpec((rows, d), lambda i: (0, 0))],
        out_specs=ANY_SPEC,
        scratch_shapes=[pltpu.VMEM((HW, rows), BF16), pltpu.VMEM((HW, d), F32), pltpu.SemaphoreType.DMA],
        compiler_params=_cparams(dimension_semantics=("arbitrary",)),
    )(dp, dp, h1)


def _weight_grad(xs, dy, name, tk=None, tn=512, k_first=0, k_tiles=None):
    rows = dy.shape[0]
    n = dy.shape[1]
    tn_ = min(tn, n)
    tk_ = xs.shape[1] if tk is None else tk
    k_tiles = xs.shape[1] // tk_ if k_tiles is None else k_tiles
    k = k_tiles * tk_

    return pl.pallas_call(
        functools.partial(_transposed_lhs_matmul), name=name, grid=(k_tiles, n // tn_),
        out_shape=jax.ShapeDtypeStruct((k, n), F32),
        in_specs=[pl.BlockSpec((rows, tk_), lambda i, j: (0, i + k_first)),
                  pl.BlockSpec((rows, tn_), lambda i, j: (0, j))],
        out_specs=pl.BlockSpec((tk_, tn_), lambda i, j: (i, j)),
        scratch_shapes=[pltpu.VMEM((tk_, rows), BF16)],
        compiler_params=_cparams(dimension_semantics=("parallel", "arbitrary")),
    )(xs, dy)


def _running_sum(x, fw):
    c = x.shape[0]
    row = lax.broadcasted_iota(jnp.int32, (c, 1), 0)
    s = 1
    while s < c:
        if fw:
            x = x + jnp.where(row >= s, pltpu.roll(x, s, axis=0), 0.0)
        else:
            x = x + jnp.where(row < c - s, pltpu.roll(x, c - s, axis=0), 0.0)
        s *= 2
    return x


def _chunk_terms(q, k, g, fw):
    c = CHUNK
    r = lax.broadcasted_iota(jnp.int32, (c, c), 0)
    s = lax.broadcasted_iota(jnp.int32, (c, c), 1)
    causal = (s <= r) if fw else (s >= r)
    causal_t = (s >= r) if fw else (s <= r)
    cum = _running_sum(g, fw)
    row = lax.broadcasted_iota(jnp.int32, (c, 1), 0)
    pos = row if fw else (c - 1 - row)
    starts = [None]
    for j in range(1, NSUB):
        rj = SUB * j - 1 if fw else c - SUB * j
        starts.append(cum[rj:rj + 1, :])
    in_blk = [(pos >= SUB * j) & (pos < SUB * (j + 1)) for j in range(NSUB)]
    e = [jnp.exp(cum)]
    for j in range(1, NSUB):
        e.append(jnp.exp(jnp.where(pos >= SUB * j, cum - starts[j], -1e30)))
    own = jnp.zeros_like(cum)
    for j in range(1, NSUB):
        own = own + jnp.where(in_blk[j], starts[j], 0.0)
    kscale = jnp.exp(own - cum)
    rend = c - 1 if fw else 0
    cend = cum[rend:rend + 1, :]
    tail = jnp.exp(cend - cum)
    qcat = jnp.concatenate([q * e[j] for j in range(NSUB)], axis=1).astype(BF16)
    kt = k * kscale
    km = jnp.concatenate([jnp.where(in_blk[j], kt, 0.0) for j in range(NSUB)], axis=1).astype(BF16)
    return dict(causal=causal, causal_t=causal_t, e=e, in_blk=in_blk, kscale=kscale, cend=cend, tail=tail,
                qcat=qcat, km=km)


def _chunk_fwd(q, k, v, g, st0, fw):
    t = _chunk_terms(q, k, g, fw)
    a = jnp.where(t["causal"], _dot_nt(t["qcat"], t["km"]), 0.0)
    o = _dot(a, v) + _dot_nt(t["qcat"][:, 0:HD], st0)
    st1 = st0 * jnp.exp(t["cend"]) + _dot_tn(v, k * t["tail"])
    return o, st1


def _chunk_bwd(q, k, v, g, st0, do, dst1, fw):
    t = _chunk_terms(q, k, g, fw)
    qcat, km, e = t["qcat"], t["km"], t["e"]
    a_t = jnp.where(t["causal_t"], _dot_nt(km, qcat), 0.0)
    ktail = k * t["tail"]
    dv = _dot(a_t, do) + _dot_nt(ktail, dst1)
    da = jnp.where(t["causal"], _dot_nt(do, v), 0.0)
    da_t = jnp.where(t["causal_t"], _dot_nt(v, do), 0.0)
    dqcat = _dot(da, km)
    dq_inter = e[0] * _dot(do, st0)
    dq = dq_inter
    for j in range(NSUB):
        dq = dq + e[j] * dqcat[:, j * HD:(j + 1) * HD]
    dkm = _dot(da_t, qcat)
    dkt = jnp.zeros_like(k)
    for j in range(NSUB):
        dkt = dkt + jnp.where(t["in_blk"][j], dkm[:, j * HD:(j + 1) * HD], 0.0)
    dk_inter = _dot(v, dst1) * t["tail"]
    dk = dkt * t["kscale"] + dk_inter
    dcum = q * dq_inter - k * dk_inter
    for j in range(NSUB):
        sl = slice(j * HD, (j + 1) * HD)
        dcum = dcum + qcat[:, sl].astype(F32) * dqcat[:, sl] - km[:, sl].astype(F32) * dkm[:, sl]
    ecend = jnp.exp(t["cend"])
    end = ecend * _colsum(st0 * dst1) + _colsum(k * dk_inter)
    dg = _running_sum(dcum, not fw) + end
    dst0 = dst1 * ecend + _dot_tn(do, q * e[0])
    return dq, dk, dv, dg, dst0


def _chunk_index(step, n_ctx_chunks, n_chunks, fw):
    if fw:
        return step
    return jnp.where(step < n_ctx_chunks, n_ctx_chunks - 1 - step, n_chunks - 1 + n_ctx_chunks - step)


def _hg_inputs(hq, hf, lbv, d_idx, sl):
    lb = _sigmoid(lbv[d_idx:d_idx + 1, sl] - lbv[2 + d_idx:3 + d_idx, sl])
    sg = _sigmoid(hf)
    f = lb + (1.0 - lb) * sg
    return _silu(hq), 1.0 - f, jnp.log(f), f, sg, lb


def _scan_fwd(p, side, n_ctx_chunks, fw, branch, name):
    rows = p.shape[0]
    n_chunks = rows // CHUNK
    d_idx = 0 if fw else 1
    hg = branch == "hg"
    cols = (C_HQ, C_HI, C_HF_FW + d_idx) if hg else (C_GQ, C_GK, C_GV)

    def body(*refs):
        if hg:
            a_ref, b_ref, c_ref, lb_ref, o_ref, st_ref, state = refs
        else:
            a_ref, b_ref, c_ref, lr_ref, wgk_ref, bgk_ref, o_ref, st_ref, state = refs
            logits = _dot(lr_ref[...], wgk_ref[...]) + bgk_ref[...]
            g_all = _log_sigmoid(logits) * (1.0 / GATE_NORM)

        @pl.when(pl.program_id(0) == 0)
        def _():
            state[...] = jnp.zeros_like(state)

        for h in range(NH):
            sl = slice(h * HD, (h + 1) * HD)
            if hg:
                q, k, g, _, _, _ = _hg_inputs(a_ref[:, sl], c_ref[:, sl], lb_ref[...], d_idx, sl)
                v = b_ref[:, sl]
            else:
                q, k, v, g = a_ref[:, sl] * (HD ** -0.5), b_ref[:, sl], c_ref[:, sl], g_all[:, sl]
            st0 = state[h]
            st_ref[0, h] = st0
            o, st1 = _chunk_fwd(q, k, v, g, st0, fw)
            o_ref[:, sl] = o
            state[h] = st1

    def cmap(blk):
        return pl.BlockSpec((CHUNK, HW), lambda j: (_chunk_index(j, n_ctx_chunks, n_chunks, fw), blk))

    fixed = lambda j: (0, 0)
    in_specs = [cmap(cols[0]), cmap(cols[1]), cmap(cols[2])]
    if hg:
        in_specs += [pl.BlockSpec((4, HW), fixed)]
        args = (p, p, p, side)
    else:
        in_specs += [pl.BlockSpec((CHUNK, 128), lambda j: (_chunk_index(j, n_ctx_chunks, n_chunks, fw), OFF_LR // 128)),
                     pl.BlockSpec((128, HW), fixed), pl.BlockSpec((1, HW), fixed)]
        args = (p, p, p, p, side[0], side[1])
    return pl.pallas_call(
        body, name=name, grid=(n_chunks,),
        out_shape=[jax.ShapeDtypeStruct((rows, HW), F32), jax.ShapeDtypeStruct((n_chunks, NH, HD, HD), F32)],
        in_specs=in_specs,
        out_specs=[pl.BlockSpec((CHUNK, HW), lambda j: (_chunk_index(j, n_ctx_chunks, n_chunks, fw), 0)),
                   pl.BlockSpec((1, NH, HD, HD), lambda j: (_chunk_index(j, n_ctx_chunks, n_chunks, fw), 0, 0, 0))],
        scratch_shapes=[pltpu.VMEM((NH, HD, HD), F32)],
        compiler_params=_cparams(dimension_semantics=("arbitrary",)),
    )(*args)


def _scan_fwd_both(p, side, n_ctx_chunks, branch, name):
    rows = p.shape[0]
    n_chunks = rows // CHUNK
    hg = branch == "hg"
    n_in = 4 if hg else 6

    def body(*refs):
        ins, outs, state = refs[:2 * n_in], refs[2 * n_in:2 * n_in + 4], refs[-1]

        @pl.when(pl.program_id(0) == 0)
        def _():
            state[...] = jnp.zeros_like(state)

        for di, fw in enumerate((True, False)):
            r = ins[di * n_in:(di + 1) * n_in]
            o_ref, st_ref = outs[2 * di], outs[2 * di + 1]
            if hg:
                a_ref, b_ref, c_ref, lb_ref = r
            else:
                a_ref, b_ref, c_ref, lr_ref, wgk_ref, bgk_ref = r
                logits = _dot(lr_ref[...], wgk_ref[...]) + bgk_ref[...]
                g_all = _log_sigmoid(logits) * (1.0 / GATE_NORM)
            for h in range(NH):
                sl = slice(h * HD, (h + 1) * HD)
                if hg:
                    q, k, g, _, _, _ = _hg_inputs(a_ref[:, sl], c_ref[:, sl], lb_ref[...], di, sl)
                    v = b_ref[:, sl]
                else:
                    q, k, v, g = a_ref[:, sl] * (HD ** -0.5), b_ref[:, sl], c_ref[:, sl], g_all[:, sl]
                st0 = state[di, h]
                st_ref[0, h] = st0
                o, st1 = _chunk_fwd(q, k, v, g, st0, fw)
                o_ref[:, sl] = o
                state[di, h] = st1

    fixed = lambda j: (0, 0)
    in_specs, args, out_specs = [], [], []
    for di, fw in enumerate((True, False)):
        chunk = functools.partial(_chunk_index, n_ctx_chunks=n_ctx_chunks, n_chunks=n_chunks, fw=fw)

        def cmap(blk, width=HW, chunk=chunk):
            return pl.BlockSpec((CHUNK, width), lambda j: (chunk(j), blk))

        if hg:
            in_specs += [cmap(C_HQ), cmap(C_HI), cmap(C_HF_FW + di), pl.BlockSpec((4, HW), fixed)]
            args += [p, p, p, side]
        else:
            in_specs += [cmap(C_GQ), cmap(C_GK), cmap(C_GV), cmap(OFF_LR // 128, 128),
                         pl.BlockSpec((128, HW), fixed), pl.BlockSpec((1, HW), fixed)]
            args += [p, p, p, p, side[di][0], side[di][1]]
        out_specs += [cmap(0), pl.BlockSpec((1, NH, HD, HD), lambda j, chunk=chunk: (chunk(j), 0, 0, 0))]
    return pl.pallas_call(
        body, name=name, grid=(n_chunks,),
        out_shape=[jax.ShapeDtypeStruct((rows, HW), F32), jax.ShapeDtypeStruct((n_chunks, NH, HD, HD), F32)] * 2,
        in_specs=in_specs, out_specs=out_specs,
        scratch_shapes=[pltpu.VMEM((2, NH, HD, HD), F32)],
        compiler_params=_cparams(dimension_semantics=("arbitrary",)),
    )(*args)


def _scan_bwd_both(p, side, states, d_o, n_ctx_chunks, branch, name):
    rows = p.shape[0]
    n_chunks = rows // CHUNK
    hg = branch == "hg"
    n_in = 6 if hg else 8
    n_out = 4 if hg else 6

    def body(*refs):
        ins, outs, dstate = refs[:2 * n_in], refs[2 * n_in:2 * n_in + 2 * n_out], refs[-1]
        first = pl.program_id(0) == 0

        @pl.when(first)
        def _():
            dstate[...] = jnp.zeros_like(dstate)

        for di, fw in enumerate((True, False)):
            r, w = ins[di * n_in:(di + 1) * n_in], outs[di * n_out:(di + 1) * n_out]
            if hg:
                a_ref, b_ref, c_ref, lb_ref, st_ref, do_ref = r
                da_ref, db_ref, dc_ref, dlb_ref = w
                acc_refs = (dlb_ref,)
            else:
                a_ref, b_ref, c_ref, lr_ref, wgk_ref, bgk_ref, st_ref, do_ref = r
                da_ref, db_ref, dc_ref, dlr_ref, dwgk_ref, dbias_ref = w
                acc_refs = (dwgk_ref, dbias_ref)
                lr = lr_ref[...]
                logits = _dot(lr, wgk_ref[...]) + bgk_ref[...]
                g_all = _log_sigmoid(logits) * (1.0 / GATE_NORM)

            @pl.when(first)
            def _(acc_refs=acc_refs):
                for ref in acc_refs:
                    ref[...] = jnp.zeros_like(ref)

            dg_parts = []
            for h in range(NH):
                sl = slice(h * HD, (h + 1) * HD)
                if hg:
                    hq, hf = a_ref[:, sl], c_ref[:, sl]
                    q, k, g, f, sg, lb = _hg_inputs(hq, hf, lb_ref[...], di, sl)
                    v = b_ref[:, sl]
                else:
                    q, k, v, g = a_ref[:, sl] * (HD ** -0.5), b_ref[:, sl], c_ref[:, sl], g_all[:, sl]
                dq, dk, dv, dg, dst0 = _chunk_bwd(q, k, v, g, st_ref[0, h], do_ref[:, sl], dstate[di, h], fw)
                dstate[di, h] = dst0
                if hg:
                    da_ref[:, sl] = dq * _dsilu(hq)
                    db_ref[:, sl] = dv
                    df = dg / f - dk
                    dc_ref[:, sl] = df * (1.0 - lb) * sg * (1.0 - sg)
                    dlb_ref[0:1, sl] += _colsum(df * (1.0 - sg))
                else:
                    da_ref[:, sl] = dq * (HD ** -0.5)
                    db_ref[:, sl] = dk
                    dc_ref[:, sl] = dv
                    dg_parts.append(dg)
            if not hg:
                dlogits = jnp.concatenate(dg_parts, axis=1) * (1.0 / GATE_NORM) * (1.0 - _sigmoid(logits))
                dlr_ref[...] = _dot_nt(dlogits, wgk_ref[...])
                dwgk_ref[...] += _dot_tn(lr, dlogits)
                dbias_ref[0:1, :] += _colsum(dlogits)

    fixed = lambda j: (0, 0)
    big = jax.ShapeDtypeStruct((rows, HW), F32)
    in_specs, args, out_shape, out_specs = [], [], [], []
    for di, fw in enumerate((True, False)):
        def chunk_of(j, fw=fw):
            return _chunk_index(n_chunks - 1 - j, n_ctx_chunks, n_chunks, fw)

        def cmap(blk, width=HW, chunk_of=chunk_of):
            return pl.BlockSpec((CHUNK, width), lambda j: (chunk_of(j), blk))

        st_spec = pl.BlockSpec((1, NH, HD, HD), lambda j, chunk_of=chunk_of: (chunk_of(j), 0, 0, 0))
        if hg:
            in_specs += [cmap(C_HQ), cmap(C_HI), cmap(C_HF_FW + di), pl.BlockSpec((4, HW), fixed), st_spec, cmap(0)]
            args += [p, p, p, side, states[di], d_o]
            out_shape += [big, big, big, jax.ShapeDtypeStruct((8, HW), F32)]
            out_specs += [cmap(0), cmap(0), cmap(0), pl.BlockSpec((8, HW), fixed)]
        else:
            in_specs += [cmap(C_GQ), cmap(C_GK), cmap(C_GV), cmap(OFF_LR // 128, 128),
                         pl.BlockSpec((128, HW), fixed), pl.BlockSpec((1, HW), fixed), st_spec, cmap(0)]
            args += [p, p, p, p, side[di][0], side[di][1], states[di], d_o]
            out_shape += [big, big, big, jax.ShapeDtypeStruct((rows, 128), F32),
                          jax.ShapeDtypeStruct((128, HW), F32), jax.ShapeDtypeStruct((8, HW), F32)]
            out_specs += [cmap(0), cmap(0), cmap(0), cmap(0, 128), pl.BlockSpec((128, HW), fixed),
                          pl.BlockSpec((8, HW), fixed)]
    return pl.pallas_call(
        body, name=name, grid=(n_chunks,), out_shape=out_shape, in_specs=in_specs, out_specs=out_specs,
        scratch_shapes=[pltpu.VMEM((2, NH, HD, HD), F32)],
        compiler_params=_cparams(dimension_semantics=("arbitrary",)),
    )(*args)


def _scan_bwd(p, side, states, d_o, n_ctx_chunks, fw, branch, name):
    rows = p.shape[0]
    n_chunks = rows // CHUNK
    d_idx = 0 if fw else 1
    hg = branch == "hg"
    cols = (C_HQ, C_HI, C_HF_FW + d_idx) if hg else (C_GQ, C_GK, C_GV)

    def body(*refs):
        if hg:
            a_ref, b_ref, c_ref, lb_ref, st_ref, do_ref, da_ref, db_ref, dc_ref, dlb_ref, dstate = refs
        else:
            (a_ref, b_ref, c_ref, lr_ref, wgk_ref, bgk_ref, st_ref, do_ref, da_ref, db_ref, dc_ref, dlr_ref,
             dwgk_ref, dbias_ref, dstate) = refs
            lr = lr_ref[...]
            logits = _dot(lr, wgk_ref[...]) + bgk_ref[...]
            g_all = _log_sigmoid(logits) * (1.0 / GATE_NORM)

        @pl.when(pl.program_id(0) == 0)
        def _():
            dstate[...] = jnp.zeros_like(dstate)
            if hg:
                dlb_ref[...] = jnp.zeros_like(dlb_ref)
            else:
                dwgk_ref[...] = jnp.zeros_like(dwgk_ref)
                dbias_ref[...] = jnp.zeros_like(dbias_ref)

        dg_parts = []
        for h in range(NH):
            sl = slice(h * HD, (h + 1) * HD)
            if hg:
                hq, hf = a_ref[:, sl], c_ref[:, sl]
                q, k, g, f, sg, lb = _hg_inputs(hq, hf, lb_ref[...], d_idx, sl)
                v = b_ref[:, sl]
            else:
                q, k, v, g = a_ref[:, sl] * (HD ** -0.5), b_ref[:, sl], c_ref[:, sl], g_all[:, sl]
            dq, dk, dv, dg, dst0 = _chunk_bwd(q, k, v, g, st_ref[0, h], do_ref[:, sl], dstate[h], fw)
            dstate[h] = dst0
            if hg:
                da_ref[:, sl] = dq * _dsilu(hq)
                db_ref[:, sl] = dv
                df = dg / f - dk
                dc_ref[:, sl] = df * (1.0 - lb) * sg * (1.0 - sg)
                dlb_ref[0:1, sl] += _colsum(df * (1.0 - sg))
            else:
                da_ref[:, sl] = dq * (HD ** -0.5)
                db_ref[:, sl] = dk
                dc_ref[:, sl] = dv
                dg_parts.append(dg)
        if not hg:
            dlogits = jnp.concatenate(dg_parts, axis=1) * (1.0 / GATE_NORM) * (1.0 - _sigmoid(logits))
            dlr_ref[...] = _dot_nt(dlogits, wgk_ref[...])
            dwgk_ref[...] += _dot_tn(lr, dlogits)
            dbias_ref[0:1, :] += _colsum(dlogits)

    def chunk_of(j):
        return _chunk_index(n_chunks - 1 - j, n_ctx_chunks, n_chunks, fw)

    def cmap(blk, width=HW):
        return pl.BlockSpec((CHUNK, width), lambda j: (chunk_of(j), blk))

    fixed = lambda j: (0, 0)
    st_spec = pl.BlockSpec((1, NH, HD, HD), lambda j: (chunk_of(j), 0, 0, 0))
    big = jax.ShapeDtypeStruct((rows, HW), F32)
    if hg:
        in_specs = [cmap(cols[0]), cmap(cols[1]), cmap(cols[2]), pl.BlockSpec((4, HW), fixed), st_spec, cmap(0)]
        args = (p, p, p, side, states, d_o)
        out_shape = [big, big, big, jax.ShapeDtypeStruct((8, HW), F32)]
        out_specs = [cmap(0), cmap(0), cmap(0), pl.BlockSpec((8, HW), fixed)]
    else:
        in_specs = [cmap(cols[0]), cmap(cols[1]), cmap(cols[2]), cmap(OFF_LR // 128, 128),
                    pl.BlockSpec((128, HW), fixed), pl.BlockSpec((1, HW), fixed), st_spec, cmap(0)]
        args = (p, p, p, p, side[0], side[1], states, d_o)
        out_shape = [big, big, big, jax.ShapeDtypeStruct((rows, 128), F32), jax.ShapeDtypeStruct((128, HW), F32),
                     jax.ShapeDtypeStruct((8, HW), F32)]
        out_specs = [cmap(0), cmap(0), cmap(0), cmap(0, 128), pl.BlockSpec((128, HW), fixed),
                     pl.BlockSpec((8, HW), fixed)]
    return pl.pallas_call(
        body, name=name, grid=(n_chunks,), out_shape=out_shape, in_specs=in_specs, out_specs=out_specs,
        scratch_shapes=[pltpu.VMEM((NH, HD, HD), F32)],
        compiler_params=_cparams(dimension_semantics=("arbitrary",)),
    )(*args)


SMALL_ROWS = 56
ROWS_MOD_X = (0, 1, 8, 16, 17, 18)
ROWS_MOD_C = (2, 3)
ROW_PRE1, ROW_POST1, ROW_ONORM, ROW_PRE2, ROW_POST2, ROW_LB, ROW_BGK, ROW_WGK = 4, 9, 10, 19, 20, 24, 32, 40
ROW_LOSS = 21


def _reduce_small(gathered, lb_full, name):
    _, _, d = gathered.shape

    def body(g_ref, lb_ref, sum_ref, dmod_ref, dbmod_ref, dlb_ref):
        total = g_ref[0]
        for b in range(1, N_DEV):
            total = total + g_ref[b]
        sum_ref[...] = total
        dmod_ref[...] = jnp.zeros_like(dmod_ref)
        for m in range(N_MOD):
            col = slice(m * d, (m + 1) * d)
            acc = jnp.zeros((1, d), F32)
            for b in range(N_DEV):
                row = g_ref[b, ROWS_MOD_X[m]:ROWS_MOD_X[m] + 1, :]
                dmod_ref[b:b + 1, col] = row
                acc = acc + row
            if m < 2:
                ctx_row = total[ROWS_MOD_C[m]:ROWS_MOD_C[m] + 1, :]
                dmod_ref[8:9, col] = ctx_row
                acc = acc + ctx_row
            dbmod_ref[:, col] = acc
        lbv = lb_ref[...]
        for dd in range(2):
            lb = _sigmoid(lbv[dd:dd + 1, :] - lbv[2 + dd:3 + dd, :])
            gl = total[ROW_LB:ROW_LB + 1, dd * HW:(dd + 1) * HW] * lb * (1.0 - lb)
            dlb_ref[dd:dd + 1, :] = gl
            dlb_ref[2 + dd:3 + dd, :] = -gl

    return pl.pallas_call(
        body, name=name,
        out_shape=[jax.ShapeDtypeStruct((SMALL_ROWS, d), F32), jax.ShapeDtypeStruct((16, N_MOD * d), F32),
                   jax.ShapeDtypeStruct((1, N_MOD * d), F32), jax.ShapeDtypeStruct((4, HW), F32)],
        in_specs=[VMEM_SPEC] * 2, out_specs=[VMEM_SPEC] * 4, compiler_params=_cparams(),
    )(gathered, lb_full)


def _c_ctx_grad(gathered, c_ctx_row, name):
    def body(g_ref, c_ref, o_ref):
        acc = g_ref[0, 0:1, :]
        for chip in range(1, N_CHIP):
            acc = acc + g_ref[2 * chip, 0:1, :]
        o_ref[...] = acc * _dsilu(c_ref[...])

    return pl.pallas_call(
        body, name=name, out_shape=jax.ShapeDtypeStruct(c_ctx_row.shape, F32),
        in_specs=[VMEM_SPEC] * 2, out_specs=VMEM_SPEC, compiler_params=_cparams(),
    )(gathered, c_ctx_row)


def _relayout_w_in(w):
    pad = jnp.zeros((w.shape[0], 128 - 2 * RANK), w.dtype)
    return jnp.concatenate([w[:, :9 * HW], w[:, 9 * HW + 2 * RANK:], w[:, 9 * HW:9 * HW + 2 * RANK], pad], axis=1)


def _relayout_w_in_rows(wt):
    pad = jnp.zeros((128 - 2 * RANK, wt.shape[1]), wt.dtype)
    return jnp.concatenate([wt[:9 * HW], wt[9 * HW + 2 * RANK:], wt[9 * HW:9 * HW + 2 * RANK], pad], axis=0)


def _w_in_grad_rows(g_main, g_lr):
    return jnp.concatenate([g_main[:9 * HW], g_lr[:2 * RANK], g_main[9 * HW:]], axis=0)


def _w_in_grad_blocks(g_main, g_lr, n_blocks):
    lr0 = 9 * HW
    n = (g_main.shape[1] + 2 * RANK) // n_blocks

    def cols(lo, hi):
        out = []
        if lo < lr0:
            out.append(g_main[:, lo:min(hi, lr0)])
        if hi > lr0 and lo < lr0 + 2 * RANK:
            out.append(g_lr[:, max(lo, lr0) - lr0:min(hi, lr0 + 2 * RANK) - lr0])
        if hi > lr0 + 2 * RANK:
            out.append(g_main[:, max(lo, lr0 + 2 * RANK) - 2 * RANK:hi - 2 * RANK])
        return out

    return jnp.stack([jnp.concatenate(cols(j * n, (j + 1) * n), axis=1) for j in range(n_blocks)])


def _blocked(full, n_blocks):
    k, n = full.shape
    return full.reshape(k, n_blocks, n // n_blocks).transpose(1, 0, 2)


def _unblocked(blocks):
    nb, k, n = blocks.shape
    return blocks.transpose(1, 0, 2).reshape(k, nb * n)


def _sample_front(x0, ctx0, modc, modx, norm_pre1, lb_full, gla_side, w_in_r):
    ctx_len = ctx0.shape[0]
    n_ctx_tiles = ctx_len // TM
    n_ctx_chunks = ctx_len // CHUNK
    h1, p = _in_projection(ctx0, x0, modc, modx, norm_pre1, w_in_r, n_ctx_tiles, "in_projection")
    o_hg_fw, st_hg_fw, o_hg_bw, st_hg_bw = _scan_fwd_both(p, lb_full, n_ctx_chunks, "hg", "scan_hg")
    o_gla_fw, st_gla_fw, o_gla_bw, st_gla_bw = _scan_fwd_both(p, gla_side, n_ctx_chunks, "gla", "scan_gla")
    return dict(h1=h1, p=p, o_list=[o_hg_fw, o_hg_bw, o_gla_fw, o_gla_bw],
                states=[st_hg_fw, st_hg_bw, st_gla_fw, st_gla_bw])


def _sample_back(reduce, front, x0, ctx0, target0, modc, modx, norm_pre1, norms, onorms, lb_full, gla_side, w_in_r,
                 wbh, wbg, wout, wg, wu, wd):
    seq, d = x0.shape
    ctx_len = ctx0.shape[0]
    n_ctx_tiles = ctx_len // TM
    n_tiles = (ctx_len + seq) // TM
    n_ctx_chunks = ctx_len // CHUNK
    h1, p, o_list = front["h1"], front["p"], front["o_list"]
    st_hg_fw, st_hg_bw, st_gla_fw, st_gla_bw = front["states"]
    z2, y1, merged, og_hg, og_gla = _mixer_tail_fwd(x0, p, o_list, modx, norms, onorms, wbh, wbg, wout, n_ctx_tiles,
                                                    "mixer_tail")
    loss_part, dz2, h2, a_act, du, dv, dy2, stat_ffn = _ffn_fwd_bwd(z2, modx, norms, wg, wu, wd, target0, "ffn")
    dff = wg.shape[0]
    tok = reduce("ffn", [_weight_grad(du, h2, "grad_w_ff_gate", tk=dff // 2, tn=d),
                         _weight_grad(dv, h2, "grad_w_ff_up", tk=dff // 2, tn=d),
                         _weight_grad(a_act, dy2, "grad_w_ff_down", tk=dff // 2)])

    (d_ohg, d_ogla, d_hgate, d_ggate, d_ghg, d_ggla, dy1, db_hg, db_gla, stat_mix) = _mixer_tail_bwd(
        x0, p, o_list, dz2, y1, modx + tok, norms, onorms, wbh, wbg, wout, n_ctx_tiles, n_tiles, "mixer_tail_bwd")
    tok = reduce("mix", [_weight_grad(og_hg, db_hg, "grad_w_br_hg"), _weight_grad(og_gla, db_gla, "grad_w_br_gla"),
                         _weight_grad(merged, dy1, "grad_w_out")])
    lb_b = lb_full + tok
    gla_b = [(wgk, bias + tok) for wgk, bias in gla_side]
    (dhq_f, dhi_f, dhf_f, dlb_f, dhq_b, dhi_b, dhf_b, dlb_b) = _scan_bwd_both(
        p, lb_b, (st_hg_fw, st_hg_bw), d_ohg, n_ctx_chunks, "hg", "scan_hg_bwd")
    (dgq_f, dgk_f, dgv_f, dlr_f, dwgk_f, dbgk_f, dgq_b, dgk_b, dgv_b, dlr_b, dwgk_b, dbgk_b) = _scan_bwd_both(
        p, gla_b, (st_gla_fw, st_gla_bw), d_ogla, n_ctx_chunks, "gla", "scan_gla_bwd")
    pieces = [dhq_f, dhq_b, dhi_f, dhi_b, dhf_f, dhf_b, d_hgate, dgq_f, dgq_b, dgk_f, dgk_b, dgv_f, dgv_b, d_ggate,
              d_ghg, d_ggla, dlr_f, dlr_b]
    dp, grad_x, stat_in = _in_projection_bwd(ctx0, x0, dz2, modc, modx, norm_pre1, w_in_r, pieces, n_ctx_tiles,
                                             "in_projection_bwd")

    reduce("small", dict(stat_in=stat_in, stat_mix=stat_mix, stat_ffn=stat_ffn, dlb=(dlb_f, dlb_b),
                         dwgk=(dwgk_f, dwgk_b), dbgk=(dbgk_f, dbgk_b)))
    reduce("in", [_w_in_grad(dp, h1, w_in_r.shape[0], "grad_w_in")])
    return dict(
        loss_part=loss_part, grad_x=grad_x, stat_in=stat_in, stat_mix=stat_mix, stat_ffn=stat_ffn,
        dlb=(dlb_f, dlb_b), dwgk=(dwgk_f, dwgk_b), dbgk=(dbgk_f, dbgk_b))


def kernel(x, c, ctx, c_ctx, w_mod, b_mod, norm_pre1, norm_post1, norm_pre2, norm_post2, w_in, hg_lb, hg_onorm, gla_w_gk, gla_b_gk, gla_onorm, w_br_hg, w_br_gla, w_out, w_ff_gate, w_ff_up, w_ff_down, loss_target, m_c_ctx, m_w_mod, m_b_mod, m_norm_pre1, m_norm_post1, m_norm_pre2, m_norm_post2, m_w_in, m_hg_lb, m_hg_onorm, m_gla_w_gk, m_gla_b_gk, m_gla_onorm, m_w_br_hg, m_w_br_gla, m_w_out, m_w_ff_gate, m_w_ff_up, m_w_ff_down, v_c_ctx, v_w_mod, v_b_mod, v_norm_pre1, v_norm_post1, v_norm_pre2, v_norm_post2, v_w_in, v_hg_lb, v_hg_onorm, v_gla_w_gk, v_gla_b_gk, v_gla_onorm, v_w_br_hg, v_w_br_gla, v_w_out, v_w_ff_gate, v_w_ff_up, v_w_ff_down):
    seq, d = x.shape[1], x.shape[2]
    ctx_len = ctx.shape[1]
    assert seq % TM == 0 and ctx_len % TM == 0 and d == 2 * HW
    ax, ay, ac = lax.axis_index("x"), lax.axis_index("y"), lax.axis_index("c")
    chip = 2 * ax + ay
    dev = 2 * chip + ac
    c_arr = jnp.reshape(ac, (1,)).astype(jnp.int32)

    nc = d // 128
    pad8 = lambda a: jnp.pad(a, ((0, -a.shape[0] % 8), (0, 0)))
    small1 = jnp.concatenate([c.reshape(nc, 128), pad8(hg_lb.reshape(4, 128)), gla_w_gk.reshape(2 * RANK, 128),
                              pad8(gla_b_gk.reshape(2, 128))], axis=0)
    got1 = _allgather8(small1, "gather_small_params")
    c_all = got1[:, :nc, :].reshape(N_DEV, d)
    per_chip = got1[0::2]
    lb_full = per_chip[:, nc:nc + 4, :].transpose(1, 0, 2).reshape(4, HW)
    wgk_full = per_chip[:, nc + 8:nc + 8 + 2 * RANK, :].transpose(1, 0, 2).reshape(2, RANK, HW)
    bgk_full = per_chip[:, nc + 8 + 2 * RANK:nc + 10 + 2 * RANK, :].transpose(1, 0, 2).reshape(2, HW)
    wgk_pad = [jnp.zeros((128, HW), F32).at[dd * RANK:(dd + 1) * RANK].set(wgk_full[dd]) for dd in range(2)]
    bgk = [bgk_full[dd:dd + 1] for dd in range(2)]

    n_mod_cols = w_mod.shape[2]
    cond = jnp.concatenate([c_all, pad8(c_ctx.reshape(1, d))], axis=0)
    b_cols = lax.dynamic_slice(b_mod, (0, chip * n_mod_cols), (1, n_mod_cols))
    mod_part = _mod_forward(cond, w_mod[0], b_cols, "mod_forward")
    mod_got = _allgather8(mod_part, "gather_mod")
    mod_all = mod_got[0::2].transpose(1, 0, 2).reshape(16, N_CHIP * n_mod_cols)
    modx = pad8(lax.dynamic_slice(mod_all, (dev, 0), (1, N_MOD * d)).reshape(N_MOD, d))
    modc = pad8(mod_all[8].reshape(N_MOD, d))

    chip_arr = jnp.reshape(chip, (1,)).astype(jnp.int32)
    transposed = ("w_in", "w_ff_gate", "w_ff_up")
    view = lambda a, nm: a[0].T if nm in transposed else a[0]
    blocks = [_cast_into_blocks(chip_arr, view(w_, nm), "cast_" + nm) for w_, nm in (
        (w_in, "w_in"), (w_br_hg, "w_br_hg"), (w_br_gla, "w_br_gla"), (w_out, "w_out"), (w_ff_gate, "w_ff_gate"),
        (w_ff_up, "w_ff_up"), (w_ff_down, "w_ff_down"))]
    gathered_in = _gather_blocks(blocks[:1], "gather_w_in", after=[mod_got])
    sems, lands, token = _blocks_start(blocks[1:], "gather_rest_start", after=[gathered_in[0]])
    w_in_r = gathered_in[0].reshape(-1, d)

    norms = jnp.concatenate([norm_pre1, norm_post1, norm_pre2, norm_post2, jnp.zeros((4, d), F32)], axis=0)
    onorms = jnp.zeros((8, d), F32).at[0, :HD].set(hg_onorm[0]).at[1, :HD].set(gla_onorm[0])
    gla_side = [(wgk_pad[dd], bgk[dd]) for dd in range(2)]
    modx = modx + token[0, 0]
    front = _sample_front(x[0], ctx[0], modc, modx, norm_pre1, lb_full, gla_side, w_in_r)
    lands = _blocks_wait(sems, lands, front["o_list"], "gather_rest_wait")
    gathered = _blocks_finish(lands, "gather_rest_finish")
    wbh, wbg = _unblocked(gathered[0]), _unblocked(gathered[1])
    wout = gathered[2].reshape(d, d)
    wg, wu, wd = (gathered[i].reshape(-1, d) for i in (3, 4, 5))
    dff = wg.shape[0]
    groups = {"ffn": ["w_ff_gate", "w_ff_up", "w_ff_down"], "mix": ["w_br_hg", "w_br_gla", "w_out"], "in": ["w_in"]}
    row_sharded = {"w_out": d // N_CHIP, "w_ff_down": dff // N_CHIP, "w_ff_gate": dff // N_CHIP,
                   "w_ff_up": dff // N_CHIP, "w_in": w_in.shape[2]}
    in_flight = {}

    small = {}

    def reduce_small(stats):
        small2 = jnp.concatenate([
            stats["stat_in"], stats["stat_mix"], stats["stat_ffn"],
            jnp.concatenate(stats["dlb"], axis=1), jnp.concatenate(stats["dbgk"], axis=1),
            jnp.concatenate([stats["dwgk"][0][0:RANK], stats["dwgk"][1][RANK:2 * RANK]], axis=1)], axis=0)
        assert small2.shape[0] == SMALL_ROWS
        got2 = _allgather8(small2, "gather_small_grads")
        total, dmod_all, g_b_mod, g_lb_full = _reduce_small(got2, lb_full, "reduce_small")
        dmod_cols = lax.dynamic_slice(dmod_all, (0, chip * n_mod_cols), (16, n_mod_cols))
        g_w_mod, cctx_part = _mod_backward(cond, w_mod[0], dmod_cols, "mod_backward")
        got3 = _allgather8(cctx_part, "gather_c_ctx_grad")
        g_c_ctx = _c_ctx_grad(got3, c_ctx.reshape(1, d), "c_ctx_grad")
        small.update(total=total, g_b_mod=g_b_mod, g_lb_full=g_lb_full, g_w_mod=g_w_mod, g_c_ctx=g_c_ctx)

    def reduce(group, grads):
        if group == "small":
            return reduce_small(grads)
        nms = groups[group]
        full = [g.reshape(N_CHIP, row_sharded[nm], d) if nm in row_sharded else _blocked(g, N_CHIP)
                for g, nm in zip(grads, nms)]
        from_sibling = _send_other_half(full, "grads_to_sibling_" + group)
        pairs = [_pair_sum(c_arr, f, r_, "pair_sum_" + nm) for f, r_, nm in zip(full, from_sibling, nms)]
        after = [small["g_c_ctx"], small["total"]] if group == "in" else []
        sems_, pairs, lands_, token_ = _scatter_start(pairs, "grads_to_owner_start_" + group, after)
        in_flight[group] = (sems_, pairs, lands_, token_)
        return token_[0, 0]

    r = _sample_back(reduce, front, x[0], ctx[0], loss_target[0], modc, modx, norm_pre1, norms, onorms, lb_full,
                     gla_side, w_in_r, wbh, wbg, wout, wg, wu, wd)
    loss_part, grad_x, stat_in, stat_mix, stat_ffn = (r[k] for k in ("loss_part", "grad_x", "stat_in", "stat_mix",
                                                                     "stat_ffn"))
    (dlb_f, dlb_b), (dwgk_f, dwgk_b), (dbgk_f, dbgk_b) = r["dlb"], r["dwgk"], r["dbgk"]

    weights = dict(w_in=(w_in, m_w_in, v_w_in), w_br_hg=(w_br_hg, m_w_br_hg, v_w_br_hg),
                   w_br_gla=(w_br_gla, m_w_br_gla, v_w_br_gla), w_out=(w_out, m_w_out, v_w_out),
                   w_ff_gate=(w_ff_gate, m_w_ff_gate, v_w_ff_gate), w_ff_up=(w_ff_up, m_w_ff_up, v_w_ff_up),
                   w_ff_down=(w_ff_down, m_w_ff_down, v_w_ff_down))
    names = ["w_in", "w_br_hg", "w_br_gla", "w_out", "w_ff_gate", "w_ff_up", "w_ff_down"]
    big = {}

    def finish(group, after):
        sems_, pairs, lands_, _ = in_flight[group]
        pairs, lands_ = _scatter_wait(sems_, pairs, lands_, after, "grads_to_owner_wait_" + group)
        own_half = [_sum_owner(chip_arr, pr, g, "chip_sum_" + nm) for pr, g, nm in zip(pairs, lands_, groups[group])]
        other_half = _swap_with_sibling(own_half, "halves_to_sibling_" + group)
        for nm, own, oth in zip(groups[group], own_half, other_half):
            w_, m_, v_ = (view(a, nm) for a in weights[nm])
            res = _adamw_halves(c_arr, own, oth, w_, m_, v_, "adamw_" + nm)
            big[nm] = [r_.T[None] if nm in transposed else r_[None] for r_ in res]
        return big[groups[group][-1]][1]

    token_in = in_flight["in"][3]
    done_ffn = finish("ffn", [token_in])
    done_mix = finish("mix", [done_ffn])

    total, g_b_mod, g_lb_full, g_w_mod, g_c_ctx = (small[k] for k in ("total", "g_b_mod", "g_lb_full", "g_w_mod",
                                                                      "g_c_ctx"))
    g_pre1, g_post1, g_pre2, g_post2 = (total[r_:r_ + 1] for r_ in (ROW_PRE1, ROW_POST1, ROW_PRE2, ROW_POST2))
    g_hg_on, g_gla_on = total[ROW_ONORM:ROW_ONORM + 1, 0:HD], total[ROW_ONORM:ROW_ONORM + 1, HD:2 * HD]
    n_lb = hg_lb.shape[2]
    g_hg_lb = lax.dynamic_slice(g_lb_full, (0, chip * n_lb), (4, n_lb))
    g_bgk = lax.dynamic_slice(total[ROW_BGK:ROW_BGK + 1].reshape(2, HW), (0, chip * n_lb), (2, n_lb))
    g_wgk_full = total[ROW_WGK:ROW_WGK + RANK].reshape(RANK, 2, HW).transpose(1, 0, 2).reshape(2 * RANK, HW)
    g_wgk = lax.dynamic_slice(g_wgk_full, (0, chip * n_lb), (2 * RANK, n_lb))

    small_items = [
        (g_c_ctx, c_ctx.reshape(1, d), m_c_ctx.reshape(1, d), v_c_ctx.reshape(1, d)),
        (g_b_mod, b_mod, m_b_mod, v_b_mod),
        (g_pre1, norm_pre1, m_norm_pre1, v_norm_pre1),
        (g_post1, norm_post1, m_norm_post1, v_norm_post1),
        (g_pre2, norm_pre2, m_norm_pre2, v_norm_pre2),
        (g_post2, norm_post2, m_norm_post2, v_norm_post2),
        (g_hg_lb, hg_lb.reshape(4, n_lb), m_hg_lb.reshape(4, n_lb), v_hg_lb.reshape(4, n_lb)),
        (g_hg_on, hg_onorm, m_hg_onorm, v_hg_onorm),
        (g_wgk, gla_w_gk.reshape(2 * RANK, n_lb), m_gla_w_gk.reshape(2 * RANK, n_lb), v_gla_w_gk.reshape(2 * RANK, n_lb)),
        (g_bgk, gla_b_gk.reshape(2, n_lb), m_gla_b_gk.reshape(2, n_lb), v_gla_b_gk.reshape(2, n_lb)),
        (g_gla_on, gla_onorm, m_gla_onorm, v_gla_onorm),
    ]
    small_res = _adamw_whole(small_items, "adamw_small")
    mod_res = _adamw_tiled(g_w_mod, w_mod[0], m_w_mod[0], v_w_mod[0], "adamw_w_mod")
    finish("in", [done_mix, mod_res[0], small_res[0][0]])

    loss = total[ROW_LOSS, 0]

    shapes = dict(c_ctx=c_ctx.shape, b_mod=b_mod.shape, norm_pre1=norm_pre1.shape, norm_post1=norm_post1.shape,
                  norm_pre2=norm_pre2.shape, norm_post2=norm_post2.shape, hg_lb=hg_lb.shape, hg_onorm=hg_onorm.shape,
                  gla_w_gk=gla_w_gk.shape, gla_b_gk=gla_b_gk.shape, gla_onorm=gla_onorm.shape)
    small_names = ["c_ctx", "b_mod", "norm_pre1", "norm_post1", "norm_pre2", "norm_post2", "hg_lb", "hg_onorm",
                   "gla_w_gk", "gla_b_gk", "gla_onorm"]
    grads, deltas, new_m, new_v = {}, {}, {}, {}
    for nm, item, res in zip(small_names, small_items, small_res):
        grads[nm] = item[0].reshape(shapes[nm])
        deltas[nm], new_m[nm], new_v[nm] = (r.reshape(shapes[nm]) for r in res)
    grads["w_mod"] = g_w_mod[None]
    deltas["w_mod"], new_m["w_mod"], new_v["w_mod"] = (r[None] for r in mod_res)
    for nm in names:
        grads[nm], deltas[nm], new_m[nm], new_v[nm] = big[nm]
    order = ["c_ctx", "w_mod", "b_mod", "norm_pre1", "norm_post1", "norm_pre2", "norm_post2", "w_in", "hg_lb",
             "hg_onorm", "gla_w_gk", "gla_b_gk", "gla_onorm", "w_br_hg", "w_br_gla", "w_out", "w_ff_gate", "w_ff_up",
             "w_ff_down"]
    return (loss, grad_x[None], *[grads[n] for n in order], *[deltas[n] for n in order],
            *[new_m[n] for n in order], *[new_v[n] for n in order])


def _weight_grad_cols(xs, dy, n_cols, name, tn=512):
    rows = dy.shape[0]
    k = tk = xs.shape[1]

    return pl.pallas_call(
        functools.partial(_transposed_lhs_matmul), name=name, grid=(k // tk, n_cols // tn),
        out_shape=jax.ShapeDtypeStruct((k, n_cols), F32),
        in_specs=[pl.BlockSpec((rows, tk), lambda i, j: (0, i)), pl.BlockSpec((rows, tn), lambda i, j: (0, j))],
        out_specs=pl.BlockSpec((tk, tn), lambda i, j: (i, j)),
        scratch_shapes=[pltpu.VMEM((tk, rows), BF16)],
        compiler_params=_cparams(dimension_semantics=("parallel", "arbitrary")),
    )(xs, dy)
```

```python
import functools

import jax
import jax.numpy as jnp
from jax import lax
from jax.experimental import pallas as pl
from jax.experimental.pallas import tpu as pltpu

F32 = jnp.float32
BF16 = jnp.bfloat16
HIGHEST = lax.Precision.HIGHEST
MESH = pl.DeviceIdType.MESH

EPS = 1e-6
CHUNK = 64
SUB = 16
NSUB = CHUNK // SUB
NH = 4
HD = 128
HW = NH * HD
RANK = 16
GATE_NORM = 16.0
N_MOD = 6
TM = 256
TM_FFN = 128
N_DEV = 8
N_CHIP = 4
VMEM_LIMIT = 56 * 1024 * 1024

ADAM_LR = 0.001
ADAM_B1 = 0.9
ADAM_B2 = 0.999
ADAM_EPS = 1e-08
ADAM_WD = 0.01
ADAM_STEP = 10

VMEM_SPEC = pl.BlockSpec(memory_space=pltpu.VMEM)
ANY_SPEC = pl.BlockSpec(memory_space=pl.ANY)
HBM_SPEC = pl.BlockSpec(memory_space=pltpu.HBM)
SEM_SPEC = pl.BlockSpec(memory_space=pltpu.SEMAPHORE)
EFFECT = pltpu.SideEffectType.DATAFLOW_SIDE_EFFECTING


def _cparams(**kw):
    return pltpu.CompilerParams(vmem_limit_bytes=VMEM_LIMIT, **kw)


def _dot(a, b):
    return jnp.dot(a.astype(BF16), b.astype(BF16), preferred_element_type=F32)


def _dot_nt(a, b):
    return lax.dot_general(a.astype(BF16), b.astype(BF16), (((1,), (1,)), ((), ())), preferred_element_type=F32)


def _dot_tn(a, b):
    return lax.dot_general(a.astype(BF16), b.astype(BF16), (((0,), (0,)), ((), ())), preferred_element_type=F32)


def _sigmoid(x):
    return 1.0 / (1.0 + jnp.exp(-x))


def _silu(x):
    return x * _sigmoid(x)


def _dsilu(x):
    s = _sigmoid(x)
    return s * (1.0 + x * (1.0 - s))


def _log_sigmoid(x):
    return jnp.minimum(x, 0.0) - jnp.log(1.0 + jnp.exp(-jnp.abs(x)))


def _colsum(a):
    return jnp.sum(a, axis=0, keepdims=True)


def _rms(a):
    r = lax.rsqrt(jnp.mean(a * a, axis=-1, keepdims=True) + EPS)
    return a * r, r


def _rms_bwd(dn, n, r):
    return r * (dn - n * jnp.mean(dn * n, axis=-1, keepdims=True))


def _place():
    x, y, c = lax.axis_index("x"), lax.axis_index("y"), lax.axis_index("c")
    chips = [(1 - x, y), (x, 1 - y), (1 - x, 1 - y)]
    return x, y, c, chips


def _allgather8(v, name):
    rows, cols = v.shape

    def body(x_ref, out_ref, send_sems, recv_sems, local_sem):
        x, y, c, chips = _place()
        me, sibling = (x, y, c), (x, y, 1 - c)

        def blk(px, py, pc):
            return out_ref.at[4 * px + 2 * py + pc]

        def copy(k, block, to, src=None):
            return pltpu.make_async_remote_copy(
                src_ref=blk(*block) if src is None else src, dst_ref=blk(*block),
                send_sem=send_sems.at[k], recv_sem=recv_sems.at[k], device_id=to, device_id_type=MESH)

        mine = pltpu.make_async_copy(x_ref, blk(*me), local_sem)
        mine.start()
        first = [copy(0, me, sibling, src=x_ref)]
        first += [copy(1 + j, me, (*chip, c), src=x_ref) for j, chip in enumerate(chips)]
        for cp in first:
            cp.start()
        passed = [copy(4 + j, (*chip, c), sibling) for j, chip in enumerate(chips)]
        for j, chip in enumerate(chips):
            copy(1 + j, (*chip, c), me).wait_recv()
            passed[j].start()
        copy(0, sibling, me).wait_recv()
        for j, chip in enumerate(chips):
            copy(4 + j, (*chip, 1 - c), me).wait_recv()
        for cp in first + passed:
            cp.wait_send()
        mine.wait()

    return pl.pallas_call(
        body, name=name,
        out_shape=jax.ShapeDtypeStruct((N_DEV, rows, cols), v.dtype),
        in_specs=[VMEM_SPEC], out_specs=VMEM_SPEC,
        scratch_shapes=[pltpu.SemaphoreType.DMA((7,)), pltpu.SemaphoreType.DMA((7,)), pltpu.SemaphoreType.DMA],
    )(v)


def _cast_into_blocks(chip_arr, w, name):
    rows, cols = w.shape
    tr = _row_tile(rows, 16, 256)

    def body(chip_ref, w_ref, o_ref):
        o_ref[0] = w_ref[...].astype(BF16)

    return pl.pallas_call(
        body, name=name,
        grid_spec=pltpu.PrefetchScalarGridSpec(
            num_scalar_prefetch=1, grid=(rows // tr,),
            in_specs=[pl.BlockSpec((tr, cols), lambda i, chip_ref: (i, 0))],
            out_specs=pl.BlockSpec((1, tr, cols), lambda i, chip_ref: (chip_ref[0], i, 0))),
        out_shape=jax.ShapeDtypeStruct((N_CHIP, rows, cols), BF16),
        compiler_params=_cparams(dimension_semantics=("parallel",)),
    )(chip_arr, w)


def _halved_by_rows(shape):
    return (shape[1] // 2) % 16 == 0


def _half_of(ref, pc, block=None):
    lead = slice(None) if block is None else block
    if _halved_by_rows(ref.shape):
        h = ref.shape[1] // 2
        return ref.at[lead, pl.ds(pl.multiple_of(pc * h, 16), h), :]
    h = ref.shape[2] // 2
    return ref.at[lead, :, pl.ds(pl.multiple_of(pc * h, 128), h)]


def _half_shape(shape):
    return (shape[0], shape[1] // 2, shape[2]) if _halved_by_rows(shape) else (shape[0], shape[1], shape[2] // 2)


def _half_rows(ref, chip_id, pc):
    return _half_of(ref, pc, chip_id)


def _gather_blocks(lands, name, after=()):
    n = len(lands)
    n_in = n + len(after)

    def body(*refs):
        outs = refs[n_in:n_in + n]
        send_sems, recv_sems = refs[n_in + n:]
        x, y, c, chips = _place()
        me_chip = 2 * x + y
        sibling = (x, y, 1 - c)

        def copy(k, j, chip_id, pc, to):
            return pltpu.make_async_remote_copy(
                src_ref=_half_rows(outs[k], chip_id, pc), dst_ref=_half_rows(outs[k], chip_id, pc),
                send_sem=send_sems.at[k, j], recv_sem=recv_sems.at[k, j], device_id=to, device_id_type=MESH)

        started = []
        for k in range(n):
            for j, chip in enumerate(chips):
                cp = copy(k, j, me_chip, c, (*chip, c))
                cp.start()
                started.append(cp)
        for k in range(n):
            for j, (px, py) in enumerate(chips):
                copy(k, j, 2 * px + py, c, sibling).wait_recv()
                cp = copy(k, 3 + j, 2 * px + py, c, sibling)
                cp.start()
                started.append(cp)
        for k in range(n):
            for j, (px, py) in enumerate(chips):
                copy(k, 3 + j, 2 * px + py, 1 - c, sibling).wait_recv()
        for cp in started:
            cp.wait_send()

    return pl.pallas_call(
        body, name=name,
        out_shape=[jax.ShapeDtypeStruct(l.shape, l.dtype) for l in lands],
        in_specs=[ANY_SPEC] * n_in, out_specs=[ANY_SPEC] * n,
        input_output_aliases={i: i for i in range(n)},
        scratch_shapes=[pltpu.SemaphoreType.DMA((n, 6)), pltpu.SemaphoreType.DMA((n, 6))],
    )(*lands, *after)


def _hbm(a):
    return pltpu.with_memory_space_constraint(a, pltpu.HBM)


def _blocks_start(lands, name, after=()):
    n = len(lands)
    n_sem = 3 * n
    first = n + len(after)

    def body(*refs):
        lnd = refs[:n]
        send_sems, recv_sems = refs[first:first + n_sem], refs[first + n_sem:first + 2 * n_sem]
        token = refs[-1]
        x, y, c, chips = _place()
        me_chip = 2 * x + y
        for k in range(n):
            for j, chip in enumerate(chips):
                pltpu.make_async_remote_copy(
                    src_ref=_half_rows(lnd[k], me_chip, c), dst_ref=_half_rows(lnd[k], me_chip, c),
                    send_sem=send_sems[3 * k + j], recv_sem=recv_sems[3 * k + j],
                    device_id=(*chip, c), device_id_type=MESH).start()
        token[...] = jnp.zeros_like(token)

    out = pl.pallas_call(
        body, name=name,
        out_shape=(*[pltpu.SemaphoreType.DMA(())] * (2 * n_sem),
                   *[pltpu.HBM(l.shape, l.dtype) for l in lands],
                   jax.ShapeDtypeStruct((8, 128), F32)),
        in_specs=[HBM_SPEC] * n + [ANY_SPEC] * len(after),
        out_specs=(*[SEM_SPEC] * (2 * n_sem), *[HBM_SPEC] * n, VMEM_SPEC),
        input_output_aliases={i: 2 * n_sem + i for i in range(n)},
        compiler_params=pltpu.CompilerParams(has_side_effects=EFFECT),
    )(*[_hbm(l) for l in lands], *after)
    return list(out[:2 * n_sem]), list(out[2 * n_sem:2 * n_sem + n]), out[-1]


def _blocks_wait(sems, lands, after, name):
    n = len(lands)
    n_sem = 3 * n

    def body(*refs):
        lnd = refs[:n]
        s_sems, r_sems = refs[n:n + n_sem], refs[n + n_sem:n + 2 * n_sem]
        x, y, c, chips = _place()
        me_chip = 2 * x + y
        for k in range(n):
            for j, (px, py) in enumerate(chips):
                cp = pltpu.make_async_remote_copy(
                    src_ref=_half_rows(lnd[k], me_chip, c), dst_ref=_half_rows(lnd[k], 2 * px + py, c),
                    send_sem=s_sems[3 * k + j], recv_sem=r_sems[3 * k + j],
                    device_id=(px, py, c), device_id_type=MESH)
                cp.wait_send()
                cp.wait_recv()

    out = pl.pallas_call(
        body, name=name,
        out_shape=tuple(pltpu.HBM(l.shape, l.dtype) for l in lands),
        in_specs=[HBM_SPEC] * n + [SEM_SPEC] * (2 * n_sem) + [ANY_SPEC] * len(after),
        out_specs=[HBM_SPEC] * n,
        input_output_aliases={i: i for i in range(n)},
        compiler_params=pltpu.CompilerParams(has_side_effects=EFFECT),
    )(*lands, *sems, *after)
    return list(out)


def _blocks_finish(lands, name):
    n = len(lands)

    def body(*refs):
        lnd = refs[n:2 * n]
        send_sems, recv_sems = refs[2 * n:]
        x, y, c, chips = _place()
        sibling = (x, y, 1 - c)

        def copy(k, j, chip_id, pc):
            return pltpu.make_async_remote_copy(
                src_ref=_half_rows(lnd[k], chip_id, pc), dst_ref=_half_rows(lnd[k], chip_id, pc),
                send_sem=send_sems.at[k, j], recv_sem=recv_sems.at[k, j], device_id=sibling, device_id_type=MESH)

        started = []
        for k in range(n):
            for j, (px, py) in enumerate(chips):
                cp = copy(k, j, 2 * px + py, c)
                cp.start()
                started.append(cp)
        for k in range(n):
            for j, (px, py) in enumerate(chips):
                copy(k, j, 2 * px + py, 1 - c).wait_recv()
        for cp in started:
            cp.wait_send()

    out = pl.pallas_call(
        body, name=name,
        out_shape=[jax.ShapeDtypeStruct(l.shape, l.dtype) for l in lands],
        in_specs=[ANY_SPEC] * n, out_specs=[ANY_SPEC] * n,
        input_output_aliases={i: i for i in range(n)},
        scratch_shapes=[pltpu.SemaphoreType.DMA((n, 3)), pltpu.SemaphoreType.DMA((n, 3))],
    )(*lands)
    return list(out)


def _gather_start(shards, name):
    n = len(shards)
    n_sem = 3 * n

    def body(*refs):
        ins, lands = refs[:n], refs[n:2 * n]
        send_sems, recv_sems = refs[2 * n:2 * n + n_sem], refs[2 * n + n_sem:2 * n + 2 * n_sem]
        token = refs[-1]
        x, y, c, chips = _place()
        me_chip = 2 * x + y
        for k in range(n):
            h = shards[k].shape[0] // 2
            rows = pl.ds(pl.multiple_of(c * h, 8), h)
            for j, chip in enumerate(chips):
                pltpu.make_async_remote_copy(
                    src_ref=ins[k].at[rows, :], dst_ref=lands[k].at[me_chip, rows, :],
                    send_sem=send_sems[3 * k + j], recv_sem=recv_sems[3 * k + j],
                    device_id=(*chip, c), device_id_type=MESH).start()
        token[...] = jnp.zeros_like(token)

    lands = [_hbm(lax.empty((N_CHIP,) + s.shape, s.dtype)) for s in shards]
    out = pl.pallas_call(
        body, name=name,
        out_shape=(*[pltpu.SemaphoreType.DMA(())] * (2 * n_sem),
                   *[pltpu.HBM(s.shape, s.dtype) for s in shards],
                   *[pltpu.HBM(l.shape, l.dtype) for l in lands],
                   jax.ShapeDtypeStruct((8, 128), F32)),
        in_specs=[HBM_SPEC] * (2 * n),
        out_specs=(*[SEM_SPEC] * (2 * n_sem), *[HBM_SPEC] * (2 * n), VMEM_SPEC),
        input_output_aliases={i: 2 * n_sem + i for i in range(2 * n)},
        compiler_params=pltpu.CompilerParams(has_side_effects=EFFECT),
    )(*[_hbm(s) for s in shards], *lands)
    sems = list(out[:2 * n_sem])
    return sems, list(out[2 * n_sem:2 * n_sem + n]), list(out[2 * n_sem + n:2 * n_sem + 2 * n]), out[-1]


def _gather_wait(sems, shards, lands, after, name):
    n = len(shards)
    n_sem = 3 * n

    def body(*refs):
        ins, lnd = refs[:n], refs[n:2 * n]
        s_sems, r_sems = refs[2 * n:2 * n + n_sem], refs[2 * n + n_sem:2 * n + 2 * n_sem]
        x, y, c, chips = _place()
        for k in range(n):
            h = shards[k].shape[0] // 2
            rows = pl.ds(pl.multiple_of(c * h, 8), h)
            for j, (px, py) in enumerate(chips):
                cp = pltpu.make_async_remote_copy(
                    src_ref=ins[k].at[rows, :], dst_ref=lnd[k].at[2 * px + py, rows, :],
                    send_sem=s_sems[3 * k + j], recv_sem=r_sems[3 * k + j],
                    device_id=(px, py, c), device_id_type=MESH)
                cp.wait_send()
                cp.wait_recv()

    out = pl.pallas_call(
        body, name=name,
        out_shape=(*[pltpu.HBM(s.shape, s.dtype) for s in shards], *[pltpu.HBM(l.shape, l.dtype) for l in lands]),
        in_specs=[HBM_SPEC] * (2 * n) + [SEM_SPEC] * (2 * n_sem) + [ANY_SPEC],
        out_specs=[HBM_SPEC] * (2 * n),
        input_output_aliases={i: i for i in range(2 * n)},
        compiler_params=pltpu.CompilerParams(has_side_effects=EFFECT),
    )(*shards, *lands, *sems, after)
    return list(out[:n]), list(out[n:])


def _gather_finish(shards, lands, name):
    n = len(shards)

    def body(*refs):
        ins, lnd = refs[:n], refs[2 * n:3 * n]
        send_sems, recv_sems, local_sems = refs[3 * n:]
        x, y, c, chips = _place()
        me_chip = 2 * x + y
        sibling = (x, y, 1 - c)

        def half(k, chip_id, pc):
            h = shards[k].shape[0] // 2
            return lnd[k].at[chip_id, pl.ds(pl.multiple_of(pc * h, 8), h), :]

        def copy(k, j, chip_id, pc):
            return pltpu.make_async_remote_copy(
                src_ref=half(k, chip_id, pc), dst_ref=half(k, chip_id, pc),
                send_sem=send_sems.at[k, j], recv_sem=recv_sems.at[k, j], device_id=sibling, device_id_type=MESH)

        locals_, started = [], []
        for k in range(n):
            cp = pltpu.make_async_copy(ins[k], lnd[k].at[me_chip], local_sems.at[k])
            cp.start()
            locals_.append(cp)
            for j, (px, py) in enumerate(chips):
                cp = copy(k, j, 2 * px + py, c)
                cp.start()
                started.append(cp)
        for k in range(n):
            for j, (px, py) in enumerate(chips):
                copy(k, j, 2 * px + py, 1 - c).wait_recv()
        for cp in started:
            cp.wait_send()
        for cp in locals_:
            cp.wait()

    out = pl.pallas_call(
        body, name=name,
        out_shape=[jax.ShapeDtypeStruct(l.shape, l.dtype) for l in lands],
        in_specs=[ANY_SPEC] * (2 * n), out_specs=[ANY_SPEC] * n,
        input_output_aliases={n + i: i for i in range(n)},
        scratch_shapes=[pltpu.SemaphoreType.DMA((n, 3)), pltpu.SemaphoreType.DMA((n, 3)),
                        pltpu.SemaphoreType.DMA((n,))],
    )(*shards, *lands)
    return list(out)


def _send_other_half(arrs, name):
    n = len(arrs)

    def body(*refs):
        ins, outs = refs[:n], refs[n:2 * n]
        send_sems, recv_sems = refs[2 * n:]
        x, y, c, _ = _place()
        cps = []
        for k in range(n):
            cp = pltpu.make_async_remote_copy(
                src_ref=_half_of(ins[k], 1 - c), dst_ref=outs[k],
                send_sem=send_sems.at[k], recv_sem=recv_sems.at[k], device_id=(x, y, 1 - c), device_id_type=MESH)
            cp.start()
            cps.append(cp)
        for cp in cps:
            cp.wait()

    return pl.pallas_call(
        body, name=name,
        out_shape=[jax.ShapeDtypeStruct(_half_shape(a.shape), a.dtype) for a in arrs],
        in_specs=[ANY_SPEC] * n, out_specs=[ANY_SPEC] * n,
        scratch_shapes=[pltpu.SemaphoreType.DMA((n,)), pltpu.SemaphoreType.DMA((n,))],
    )(*arrs)


def _blocks_to_owner(arrs, name):
    n = len(arrs)

    def body(*refs):
        ins, outs = refs[:n], refs[n:2 * n]
        send_sems, recv_sems, local_sems = refs[2 * n:]
        x, y, c, chips = _place()
        me_chip = 2 * x + y
        locals_, started = [], []
        for k in range(n):
            cp = pltpu.make_async_copy(ins[k].at[me_chip], outs[k].at[me_chip], local_sems.at[k])
            cp.start()
            locals_.append(cp)

        def copy(k, j, src_block, dst_slot, to):
            return pltpu.make_async_remote_copy(
                src_ref=ins[k].at[src_block], dst_ref=outs[k].at[dst_slot],
                send_sem=send_sems.at[k, j], recv_sem=recv_sems.at[k, j], device_id=to, device_id_type=MESH)

        for k in range(n):
            for j, (px, py) in enumerate(chips):
                cp = copy(k, j, 2 * px + py, me_chip, (px, py, c))
                cp.start()
                started.append(cp)
        for k in range(n):
            for j, (px, py) in enumerate(chips):
                copy(k, j, me_chip, 2 * px + py, (px, py, c)).wait_recv()
        for cp in started:
            cp.wait_send()
        for cp in locals_:
            cp.wait()

    return pl.pallas_call(
        body, name=name,
        out_shape=[jax.ShapeDtypeStruct(a.shape, a.dtype) for a in arrs],
        in_specs=[ANY_SPEC] * n, out_specs=[ANY_SPEC] * n,
        scratch_shapes=[pltpu.SemaphoreType.DMA((n, 3)), pltpu.SemaphoreType.DMA((n, 3)),
                        pltpu.SemaphoreType.DMA((n,))],
    )(*arrs)


def _scatter_blocks(arrs, name):
    n = len(arrs)

    def body(*refs):
        ins, outs = refs[:n], refs[n:2 * n]
        send_sems, recv_sems = refs[2 * n:]
        x, y, c, chips = _place()
        me_chip = 2 * x + y

        def copy(k, j, src_block, dst_slot, to):
            return pltpu.make_async_remote_copy(
                src_ref=ins[k].at[src_block], dst_ref=outs[k].at[dst_slot],
                send_sem=send_sems.at[k, j], recv_sem=recv_sems.at[k, j], device_id=to, device_id_type=MESH)

        started = []
        for k in range(n):
            for j, (px, py) in enumerate(chips):
                cp = copy(k, j, 2 * px + py, me_chip, (px, py, c))
                cp.start()
                started.append(cp)
        for k in range(n):
            for j, (px, py) in enumerate(chips):
                copy(k, j, me_chip, 2 * px + py, (px, py, c)).wait_recv()
        for cp in started:
            cp.wait_send()

    return pl.pallas_call(
        body, name=name,
        out_shape=[jax.ShapeDtypeStruct(a.shape, a.dtype) for a in arrs],
        in_specs=[ANY_SPEC] * n, out_specs=[ANY_SPEC] * n,
        scratch_shapes=[pltpu.SemaphoreType.DMA((n, 3)), pltpu.SemaphoreType.DMA((n, 3))],
    )(*arrs)


def _scatter_start(arrs, name, after=()):
    n = len(arrs)
    n_sem = 3 * n
    first = 2 * n + len(after)

    def body(*refs):
        ins, lnd = refs[:n], refs[n:2 * n]
        send_sems, recv_sems = refs[first:first + n_sem], refs[first + n_sem:first + 2 * n_sem]
        token = refs[-1]
        x, y, c, chips = _place()
        me_chip = 2 * x + y
        for k in range(n):
            for j, (px, py) in enumerate(chips):
                pltpu.make_async_remote_copy(
                    src_ref=ins[k].at[2 * px + py], dst_ref=lnd[k].at[me_chip],
                    send_sem=send_sems[3 * k + j], recv_sem=recv_sems[3 * k + j],
                    device_id=(px, py, c), device_id_type=MESH).start()
        token[...] = jnp.zeros_like(token)

    lands = [_hbm(lax.empty(a.shape, a.dtype)) for a in arrs]
    out = pl.pallas_call(
        body, name=name,
        out_shape=(*[pltpu.SemaphoreType.DMA(())] * (2 * n_sem),
                   *[pltpu.HBM(a.shape, a.dtype) for a in arrs], *[pltpu.HBM(a.shape, a.dtype) for a in arrs],
                   jax.ShapeDtypeStruct((8, 128), F32)),
        in_specs=[HBM_SPEC] * (2 * n) + [ANY_SPEC] * len(after),
        out_specs=(*[SEM_SPEC] * (2 * n_sem), *[HBM_SPEC] * (2 * n), VMEM_SPEC),
        input_output_aliases={i: 2 * n_sem + i for i in range(2 * n)},
        compiler_params=pltpu.CompilerParams(has_side_effects=EFFECT),
    )(*[_hbm(a) for a in arrs], *lands, *after)
    base = 2 * n_sem
    return list(out[:base]), list(out[base:base + n]), list(out[base + n:base + 2 * n]), out[-1]


def _scatter_wait(sems, arrs, lands, after, name):
    n = len(arrs)
    n_sem = 3 * n

    def body(*refs):
        ins, lnd = refs[:n], refs[n:2 * n]
        s_sems, r_sems = refs[2 * n:2 * n + n_sem], refs[2 * n + n_sem:2 * n + 2 * n_sem]
        x, y, c, chips = _place()
        for k in range(n):
            for j, (px, py) in enumerate(chips):
                cp = pltpu.make_async_remote_copy(
                    src_ref=ins[k].at[2 * px + py], dst_ref=lnd[k].at[2 * px + py],
                    send_sem=s_sems[3 * k + j], recv_sem=r_sems[3 * k + j],
                    device_id=(px, py, c), device_id_type=MESH)
                cp.wait_send()
                cp.wait_recv()

    out = pl.pallas_call(
        body, name=name,
        out_shape=tuple(pltpu.HBM(a.shape, a.dtype) for a in list(arrs) + list(lands)),
        in_specs=[HBM_SPEC] * (2 * n) + [SEM_SPEC] * (2 * n_sem) + [ANY_SPEC] * len(after),
        out_specs=[HBM_SPEC] * (2 * n),
        input_output_aliases={i: i for i in range(2 * n)},
        compiler_params=pltpu.CompilerParams(has_side_effects=EFFECT),
    )(*arrs, *lands, *sems, *after)
    return list(out[:n]), list(out[n:])


def _sum_owner(chip_arr, pairs, got, name):
    nb, h, cols = got.shape
    tr = _row_tile(h, 16, 256)

    def body(chip_ref, own_ref, a_ref, b_ref, c_ref, o_ref):
        o_ref[...] = ((own_ref[0].astype(F32) + a_ref[0].astype(F32)) + b_ref[0].astype(F32)) + c_ref[0].astype(F32)

    def slot(off):
        return pl.BlockSpec((1, tr, cols), lambda i, chip_ref: ((chip_ref[0] + off) % N_CHIP, i, 0))

    return pl.pallas_call(
        body, name=name,
        grid_spec=pltpu.PrefetchScalarGridSpec(
            num_scalar_prefetch=1, grid=(h // tr,),
            in_specs=[slot(0), slot(1), slot(2), slot(3)],
            out_specs=pl.BlockSpec((tr, cols), lambda i, chip_ref: (i, 0))),
        out_shape=jax.ShapeDtypeStruct((h, cols), F32),
        compiler_params=_cparams(dimension_semantics=("parallel",)),
    )(chip_arr, pairs, got, got, got)


def _swap_with_sibling(arrs, name):
    n = len(arrs)

    def body(*refs):
        ins, outs = refs[:n], refs[n:2 * n]
        send_sems, recv_sems = refs[2 * n:]
        x, y, c, _ = _place()
        cps = []
        for k in range(n):
            cp = pltpu.make_async_remote_copy(
                src_ref=ins[k], dst_ref=outs[k], send_sem=send_sems.at[k], recv_sem=recv_sems.at[k],
                device_id=(x, y, 1 - c), device_id_type=MESH)
            cp.start()
            cps.append(cp)
        for cp in cps:
            cp.wait()

    return pl.pallas_call(
        body, name=name,
        out_shape=[jax.ShapeDtypeStruct(a.shape, a.dtype) for a in arrs],
        in_specs=[ANY_SPEC] * n, out_specs=[ANY_SPEC] * n,
        scratch_shapes=[pltpu.SemaphoreType.DMA((n,)), pltpu.SemaphoreType.DMA((n,))],
    )(*arrs)


def _row_tile(h, mult=8, cap=128):
    for t in range(cap - cap % mult, mult - 1, -mult):
        if h % t == 0:
            return t
    if mult > 8:
        return _row_tile(h, 8, cap)
    raise ValueError(h)


def _cast_bf16(a, name):
    rows, cols = a.shape
    tr = _row_tile(rows, 16, 256)

    def body(a_ref, o_ref):
        o_ref[...] = a_ref[...].astype(BF16)

    return pl.pallas_call(
        body, name=name, grid=(rows // tr,),
        out_shape=jax.ShapeDtypeStruct(a.shape, BF16),
        in_specs=[pl.BlockSpec((tr, cols), lambda i: (i, 0))],
        out_specs=pl.BlockSpec((tr, cols), lambda i: (i, 0)),
        compiler_params=_cparams(dimension_semantics=("parallel",)),
    )(a)


def _pair_sum(c_arr, full, recv, name):
    nb, rows, cols = full.shape

    def body(c_ref, f_ref, r_ref, o_ref):
        o_ref[...] = (f_ref[...] + r_ref[...]).astype(BF16)

    if _halved_by_rows(full.shape):
        h = rows // 2
        tr = _row_tile(h, 16, 256)
        steps = h // tr
        own = pl.BlockSpec((1, tr, cols), lambda b, i, c_ref: (b, c_ref[0] * steps + i, 0))
        half = pl.BlockSpec((1, tr, cols), lambda b, i, c_ref: (b, i, 0))
    else:
        steps = 1
        own = pl.BlockSpec((1, rows, cols // 2), lambda b, i, c_ref: (b, 0, c_ref[0]))
        half = pl.BlockSpec((1, rows, cols // 2), lambda b, i, c_ref: (b, 0, 0))
    return pl.pallas_call(
        body, name=name,
        grid_spec=pltpu.PrefetchScalarGridSpec(
            num_scalar_prefetch=1, grid=(nb, steps), in_specs=[own, half], out_specs=half),
        out_shape=jax.ShapeDtypeStruct(_half_shape(full.shape), BF16),
        compiler_params=_cparams(dimension_semantics=("parallel", "parallel")),
    )(c_arr, full, recv)


def _sum_chips(got, name):
    nb, h, cols = got.shape
    tr = _row_tile(h, 16, 256)

    def body(g_ref, o_ref):
        g = g_ref[...].astype(F32)
        o_ref[...] = ((g[0] + g[1]) + g[2]) + g[3]

    return pl.pallas_call(
        body, name=name, grid=(h // tr,),
        out_shape=jax.ShapeDtypeStruct((h, cols), F32),
        in_specs=[pl.BlockSpec((nb, tr, cols), lambda i: (0, i, 0))],
        out_specs=pl.BlockSpec((tr, cols), lambda i: (i, 0)),
        compiler_params=_cparams(dimension_semantics=("parallel",)),
    )(got)


def _adam_math(g, w, m, v):
    m1 = ADAM_B1 * m + (1.0 - ADAM_B1) * g
    v1 = ADAM_B2 * v + (1.0 - ADAM_B2) * (g * g)
    m_hat = m1 / (1.0 - ADAM_B1 ** ADAM_STEP)
    v_hat = v1 / (1.0 - ADAM_B2 ** ADAM_STEP)
    delta = -ADAM_LR * (m_hat / (jnp.sqrt(v_hat) + ADAM_EPS) + ADAM_WD * w)
    return delta, m1, v1


def _adamw_halves(c_arr, own, other, w, m, v, name):
    rows, cols = w.shape

    def body(c_ref, own_ref, oth_ref, w_ref, m_ref, v_ref, g_out, d_out, m_out, v_out):
        g = jnp.where(pl.program_id(0) == c_ref[0], own_ref[...], oth_ref[...])
        d, m1, v1 = _adam_math(g, w_ref[...], m_ref[...], v_ref[...])
        g_out[...] = g
        d_out[...] = d
        m_out[...] = m1
        v_out[...] = v1

    if own.shape[1] == cols:
        h = rows // 2
        tr = _row_tile(h)
        steps = h // tr
        half_spec = pl.BlockSpec((tr, cols), lambda p, i, c_ref: (i, 0))
        full_spec = pl.BlockSpec((tr, cols), lambda p, i, c_ref: (p * steps + i, 0))
    else:
        tr = _row_tile(rows)
        steps = rows // tr
        half_spec = pl.BlockSpec((tr, cols // 2), lambda p, i, c_ref: (i, 0))
        full_spec = pl.BlockSpec((tr, cols // 2), lambda p, i, c_ref: (i, p))
    return pl.pallas_call(
        body, name=name,
        grid_spec=pltpu.PrefetchScalarGridSpec(
            num_scalar_prefetch=1, grid=(2, steps),
            in_specs=[half_spec, half_spec, full_spec, full_spec, full_spec],
            out_specs=[full_spec] * 4),
        out_shape=[jax.ShapeDtypeStruct(w.shape, F32)] * 4,
        compiler_params=_cparams(dimension_semantics=("parallel", "parallel")),
    )(c_arr, own, other, w, m, v)


def _adamw_whole(items, name):
    n = len(items)

    def body(*refs):
        ins, outs = refs[:4 * n], refs[4 * n:]
        for k in range(n):
            g, w, m, v = (r[...] for r in ins[4 * k:4 * k + 4])
            d, m1, v1 = _adam_math(g, w, m, v)
            outs[3 * k][...] = d
            outs[3 * k + 1][...] = m1
            outs[3 * k + 2][...] = v1

    flat = [a for it in items for a in it]
    shapes = [jax.ShapeDtypeStruct(it[1].shape, F32) for it in items for _ in range(3)]
    out = pl.pallas_call(
        body, name=name, out_shape=shapes,
        in_specs=[VMEM_SPEC] * (4 * n), out_specs=[VMEM_SPEC] * (3 * n),
        compiler_params=_cparams(),
    )(*flat)
    return [tuple(out[3 * k:3 * k + 3]) for k in range(n)]


def _adamw_tiled(g, w, m, v, name):
    rows, cols = w.shape
    tr = _row_tile(rows)

    def body(g_ref, w_ref, m_ref, v_ref, d_out, m_out, v_out):
        d, m1, v1 = _adam_math(g_ref[...], w_ref[...], m_ref[...], v_ref[...])
        d_out[...] = d
        m_out[...] = m1
        v_out[...] = v1

    spec = pl.BlockSpec((tr, cols), lambda i: (i, 0))
    return pl.pallas_call(
        body, name=name, grid=(rows // tr,),
        out_shape=[jax.ShapeDtypeStruct(w.shape, F32)] * 3,
        in_specs=[spec] * 4, out_specs=[spec] * 3,
        compiler_params=_cparams(dimension_semantics=("parallel",)),
    )(g, w, m, v)


def _mod_forward(cond, w_mod, b_mod_cols, name):
    def body(c_ref, w_ref, b_ref, o_ref):
        o_ref[...] = _dot(_silu(c_ref[...]), w_ref[...]) + b_ref[...]

    return pl.pallas_call(
        body, name=name, out_shape=jax.ShapeDtypeStruct((cond.shape[0], w_mod.shape[1]), F32),
        in_specs=[VMEM_SPEC] * 3, out_specs=VMEM_SPEC, compiler_params=_cparams(),
    )(cond, w_mod, b_mod_cols)


def _mod_backward(cond, w_mod, dmod_cols, name):
    def body(c_ref, w_ref, d_ref, gw_ref, gc_ref):
        s = _silu(c_ref[...])
        d = d_ref[...]
        gw_ref[...] = _dot_tn(s, d)
        gc_ref[...] = _dot_nt(d[8:16, :], w_ref[...])

    return pl.pallas_call(
        body, name=name,
        out_shape=[jax.ShapeDtypeStruct(w_mod.shape, F32), jax.ShapeDtypeStruct((8, w_mod.shape[0]), F32)],
        in_specs=[VMEM_SPEC] * 3, out_specs=[VMEM_SPEC] * 2, compiler_params=_cparams(),
    )(cond, w_mod, dmod_cols)


def _col_chunks(width, step=512):
    return [(s, min(step, width - s)) for s in range(0, width, step)]


def _w_in_row(p_off):
    if p_off < 9 * HW:
        return p_off
    return 9 * HW if p_off == OFF_LR else p_off + 2 * RANK


def _in_projection(ctx0, x0, modc, modx, pre1, w_t, n_ctx_tiles, name):
    d = x0.shape[1]
    rows = ctx0.shape[0] + x0.shape[0]
    width = P_WIDTH

    def body(ctx_ref, x_ref, modc_ref, modx_ref, pre_ref, w_ref, h_ref, p_ref):
        is_ctx = pl.program_id(0) < n_ctx_tiles
        n, _ = _rms(jnp.where(is_ctx, ctx_ref[...], x_ref[...]))
        shift = jnp.where(is_ctx, modc_ref[0:1, :], modx_ref[0:1, :])
        scale = jnp.where(is_ctx, modc_ref[1:2, :], modx_ref[1:2, :])
        h = (n * pre_ref[...] * (1.0 + scale) + shift).astype(BF16)
        h_ref[...] = h
        for s, w in _col_chunks(width):
            p_ref[:, s:s + w] = _dot_nt(h, w_ref[_w_in_row(s):_w_in_row(s) + w, :])

    row = lambda i: (i, 0)
    fixed = lambda i: (0, 0)
    return pl.pallas_call(
        body, name=name, grid=(rows // TM,),
        out_shape=[jax.ShapeDtypeStruct((rows, d), BF16), jax.ShapeDtypeStruct((rows, width), F32)],
        in_specs=[pl.BlockSpec((TM, d), lambda i: (jnp.minimum(i, n_ctx_tiles - 1), 0)),
                  pl.BlockSpec((TM, d), lambda i: (jnp.maximum(i - n_ctx_tiles, 0), 0)),
                  pl.BlockSpec((8, d), fixed), pl.BlockSpec((8, d), fixed), pl.BlockSpec((1, d), fixed), VMEM_SPEC],
        out_specs=[pl.BlockSpec((TM, d), row), pl.BlockSpec((TM, width), row)],
        compiler_params=_cparams(dimension_semantics=("parallel",)),
    )(ctx0, x0, modc, modx, pre1, w_t)


C_HQ, C_HI, C_HF_FW, C_HF_BW, C_HGATE, C_GQ, C_GK, C_GV, C_GGATE = range(9)
OFF_GATE_HG = 9 * HW
OFF_LR = 13 * HW
P_WIDTH = OFF_LR + 128


def _head_norm_fwd(o, w):
    outs, ns, rs = [], [], []
    for h in range(NH):
        n, r = _rms(o[:, h * HD:(h + 1) * HD])
        ns.append(n)
        rs.append(r)
        outs.append(n * w)
    return jnp.concatenate(outs, axis=1), ns, rs


def _mixer_tail(z, o_hg, o_gla, p_hgate, p_ggate, p_gate_hg, p_gate_gla, hg_on, gla_on, wbh, wbg, wout):
    on_hg, n_hg, r_hg = _head_norm_fwd(o_hg, hg_on)
    on_gla, n_gla, r_gla = _head_norm_fwd(o_gla, gla_on)
    og_hg = (on_hg * _silu(p_hgate)).astype(BF16)
    og_gla = (on_gla * _silu(p_ggate)).astype(BF16)
    b_hg = jnp.dot(og_hg, wbh, preferred_element_type=F32)
    b_gla = jnp.dot(og_gla, wbg, preferred_element_type=F32)
    s_hg = _sigmoid(p_gate_hg)
    s_gla = _sigmoid(p_gate_gla)
    merged = (s_hg * b_hg + s_gla * b_gla).astype(BF16)
    y1 = jnp.dot(merged, wout, preferred_element_type=F32)
    return dict(on_hg=on_hg, n_hg=n_hg, r_hg=r_hg, on_gla=on_gla, n_gla=n_gla, r_gla=r_gla, og_hg=og_hg,
                og_gla=og_gla, b_hg=b_hg, b_gla=b_gla, s_hg=s_hg, s_gla=s_gla, merged=merged, y1=y1)


def _mixer_ffn(x_lat, p, o_list, modx, norms, onorms, w_br_hg, w_br_gla, w_out, w_gate, w_up, w_down, target,
               n_ctx_tiles, name):
    rows, d = x_lat.shape
    dff = w_gate.shape[0]
    inv_d = 1.0 / d

    def body(x_ref, ofw_hg, obw_hg, ofw_gla, obw_gla, p_hgate, p_ggate, p_ghg_a, p_ghg_b, p_ggla_a, p_ggla_b,
             modx_ref, norm_ref, on_ref, wbh_ref, wbg_ref, wout_ref, wg_ref, wu_ref, wd_ref, t_ref,
             loss_ref, dz2_ref, y1_ref, mrg_ref, oghg_ref, oggla_ref, h2_ref, a_ref, du_ref, dv_ref, dy2_ref,
             stat_ref):
        i = pl.program_id(0)
        post1, pre2, post2 = norm_ref[1:2, :], norm_ref[2:3, :], norm_ref[3:4, :]
        gate1, shift2, scale2, gate2 = modx_ref[2:3, :], modx_ref[3:4, :], modx_ref[4:5, :], modx_ref[5:6, :]
        p_gate_hg = jnp.concatenate([p_ghg_a[...], p_ghg_b[...]], axis=1)
        p_gate_gla = jnp.concatenate([p_ggla_a[...], p_ggla_b[...]], axis=1)
        t = _mixer_tail(x_ref[...], ofw_hg[...] + obw_hg[...], ofw_gla[...] + obw_gla[...], p_hgate[...],
                        p_ggate[...], p_gate_hg, p_gate_gla, on_ref[0:1, 0:HD], on_ref[1:2, 0:HD],
                        wbh_ref[...], wbg_ref[...], wout_ref[...])
        y1_ref[...] = t["y1"]
        mrg_ref[...] = t["merged"]
        oghg_ref[...] = t["og_hg"]
        oggla_ref[...] = t["og_gla"]
        n1, _ = _rms(t["y1"])
        z2 = x_ref[...] + n1 * post1 * gate1
        n2, r2 = _rms(z2)
        nw2 = n2 * pre2
        h2 = (nw2 * (1.0 + scale2) + shift2).astype(BF16)
        h2_ref[...] = h2
        u = _dot_nt(h2, wg_ref[...])
        v = _dot_nt(h2, wu_ref[...])
        su = _silu(u)
        a = (su * v).astype(BF16)
        a_ref[...] = a
        y2 = jnp.dot(a, wd_ref[...], preferred_element_type=F32)
        n3, r3 = _rms(y2)
        z3 = z2 + n3 * post2 * gate2
        err = z3 - t_ref[...]
        part = 0.5 * inv_d * jnp.sum(err * err)
        dz3 = err * inv_d
        dgate2 = _colsum(dz3 * n3 * post2)
        tt = dz3 * gate2
        dpost2 = _colsum(tt * n3)
        dy2 = _rms_bwd(tt * post2, n3, r3).astype(BF16)
        dy2_ref[...] = dy2
        da = _dot_nt(dy2, wd_ref[...])
        du = (da * v * _dsilu(u)).astype(BF16)
        dv = (da * su).astype(BF16)
        du_ref[...] = du
        dv_ref[...] = dv
        dh2 = (jnp.dot(du, wg_ref[...], preferred_element_type=F32)
               + jnp.dot(dv, wu_ref[...], preferred_element_type=F32))
        dshift2 = _colsum(dh2)
        dscale2 = _colsum(dh2 * nw2)
        dnw2 = dh2 * (1.0 + scale2)
        dpre2 = _colsum(dnw2 * n2)
        dz2_ref[...] = dz3 + _rms_bwd(dnw2 * pre2, n2, r2)

        @pl.when(i == 0)
        def _():
            stat_ref[...] = jnp.zeros_like(stat_ref)
            loss_ref[...] = jnp.zeros_like(loss_ref)

        for r, val in enumerate((dshift2, dscale2, dgate2, dpre2, dpost2)):
            stat_ref[r:r + 1, :] += val
        loss_ref[...] += part
        stat_ref[5:6, 0:128] += part

    tm = TM_FFN
    ctx_tiles = n_ctx_tiles * (TM // tm)
    lat = lambda i: (i, 0)
    full = lambda i: (i + ctx_tiles, 0)
    fixed = lambda i: (0, 0)

    def pcol(blk):
        return pl.BlockSpec((tm, HW), lambda i: (i + ctx_tiles, blk))

    in_specs = ([pl.BlockSpec((tm, d), lat)] + [pl.BlockSpec((tm, HW), full)] * 4
                + [pcol(C_HGATE), pcol(C_GGATE), pcol(9), pcol(10), pcol(11), pcol(12)]
                + [pl.BlockSpec((8, d), fixed), pl.BlockSpec((8, d), fixed), pl.BlockSpec((8, d), fixed)]
                + [VMEM_SPEC] * 6 + [pl.BlockSpec((tm, d), lat)])
    bf = lambda w: jax.ShapeDtypeStruct((rows, w), BF16)
    out_shape = [jax.ShapeDtypeStruct((8, 128), F32), jax.ShapeDtypeStruct((rows, d), F32),
                 jax.ShapeDtypeStruct((rows, d), F32), bf(d), bf(HW), bf(HW), bf(d), bf(dff), bf(dff), bf(dff), bf(d),
                 jax.ShapeDtypeStruct((8, d), F32)]
    out_specs = [pl.BlockSpec((8, 128), fixed), pl.BlockSpec((tm, d), lat), pl.BlockSpec((tm, d), lat),
                 pl.BlockSpec((tm, d), lat), pl.BlockSpec((tm, HW), lat), pl.BlockSpec((tm, HW), lat),
                 pl.BlockSpec((tm, d), lat), pl.BlockSpec((tm, dff), lat), pl.BlockSpec((tm, dff), lat),
                 pl.BlockSpec((tm, dff), lat), pl.BlockSpec((tm, d), lat), pl.BlockSpec((8, d), fixed)]
    return pl.pallas_call(
        body, name=name, grid=(rows // tm,), out_shape=out_shape, in_specs=in_specs, out_specs=out_specs,
        compiler_params=_cparams(dimension_semantics=("arbitrary",)),
    )(x_lat, *o_list, p, p, p, p, p, p, modx, norms, onorms, w_br_hg, w_br_gla, w_out, w_gate, w_up, w_down, target)


def _mixer_tail_fwd(x_lat, p, o_list, modx, norms, onorms, w_br_hg, w_br_gla, w_out, n_ctx_tiles, name):
    rows, d = x_lat.shape

    def body(x_ref, ofw_hg, obw_hg, ofw_gla, obw_gla, p_hgate, p_ggate, p_ghg_a, p_ghg_b, p_ggla_a, p_ggla_b,
             modx_ref, norm_ref, on_ref, wbh_ref, wbg_ref, wout_ref, z2_ref, y1_ref, mrg_ref, oghg_ref, oggla_ref):
        p_gate_hg = jnp.concatenate([p_ghg_a[...], p_ghg_b[...]], axis=1)
        p_gate_gla = jnp.concatenate([p_ggla_a[...], p_ggla_b[...]], axis=1)
        t = _mixer_tail(x_ref[...], ofw_hg[...] + obw_hg[...], ofw_gla[...] + obw_gla[...], p_hgate[...],
                        p_ggate[...], p_gate_hg, p_gate_gla, on_ref[0:1, 0:HD], on_ref[1:2, 0:HD],
                        wbh_ref[...], wbg_ref[...], wout_ref[...])
        y1_ref[...] = t["y1"]
        mrg_ref[...] = t["merged"]
        oghg_ref[...] = t["og_hg"]
        oggla_ref[...] = t["og_gla"]
        n1, _ = _rms(t["y1"])
        z2_ref[...] = x_ref[...] + n1 * norm_ref[1:2, :] * modx_ref[2:3, :]

    lat = lambda i: (i, 0)
    full = lambda i: (i + n_ctx_tiles, 0)
    fixed = lambda i: (0, 0)

    def pcol(blk):
        return pl.BlockSpec((TM, HW), lambda i: (i + n_ctx_tiles, blk))

    in_specs = ([pl.BlockSpec((TM, d), lat)] + [pl.BlockSpec((TM, HW), full)] * 4
                + [pcol(C_HGATE), pcol(C_GGATE), pcol(9), pcol(10), pcol(11), pcol(12)]
                + [pl.BlockSpec((8, d), fixed)] * 3 + [VMEM_SPEC] * 3)
    bf = lambda w: jax.ShapeDtypeStruct((rows, w), BF16)
    f32 = jax.ShapeDtypeStruct((rows, d), F32)
    return pl.pallas_call(
        body, name=name, grid=(rows // TM,), out_shape=[f32, f32, bf(d), bf(HW), bf(HW)], in_specs=in_specs,
        out_specs=[pl.BlockSpec((TM, d), lat)] * 3 + [pl.BlockSpec((TM, HW), lat)] * 2,
        compiler_params=_cparams(dimension_semantics=("parallel",)),
    )(x_lat, *o_list, p, p, p, p, p, p, modx, norms, onorms, w_br_hg, w_br_gla, w_out)


def _ffn_fwd_bwd(z2, modx, norms, w_gate, w_up, w_down, target, name):
    rows, d = z2.shape
    dff = w_gate.shape[0]
    inv_d = 1.0 / d

    def body(z2_ref, modx_ref, norm_ref, wg_ref, wu_ref, wd_ref, t_ref,
             loss_ref, dz2_ref, h2_ref, a_ref, du_ref, dv_ref, dy2_ref, stat_ref):
        i = pl.program_id(0)
        pre2, post2 = norm_ref[2:3, :], norm_ref[3:4, :]
        shift2, scale2, gate2 = modx_ref[3:4, :], modx_ref[4:5, :], modx_ref[5:6, :]
        z2 = z2_ref[...]
        n2, r2 = _rms(z2)
        nw2 = n2 * pre2
        h2 = (nw2 * (1.0 + scale2) + shift2).astype(BF16)
        h2_ref[...] = h2
        u = _dot_nt(h2, wg_ref[...])
        v = _dot_nt(h2, wu_ref[...])
        su = _silu(u)
        a = (su * v).astype(BF16)
        a_ref[...] = a
        y2 = jnp.dot(a, wd_ref[...], preferred_element_type=F32)
        n3, r3 = _rms(y2)
        err = z2 + n3 * post2 * gate2 - t_ref[...]
        part = 0.5 * inv_d * jnp.sum(err * err)
        dz3 = err * inv_d
        dgate2 = _colsum(dz3 * n3 * post2)
        tt = dz3 * gate2
        dpost2 = _colsum(tt * n3)
        dy2 = _rms_bwd(tt * post2, n3, r3).astype(BF16)
        dy2_ref[...] = dy2
        da = _dot_nt(dy2, wd_ref[...])
        du = (da * v * _dsilu(u)).astype(BF16)
        dv = (da * su).astype(BF16)
        du_ref[...] = du
        dv_ref[...] = dv
        dh2 = (jnp.dot(du, wg_ref[...], preferred_element_type=F32)
               + jnp.dot(dv, wu_ref[...], preferred_element_type=F32))
        dshift2 = _colsum(dh2)
        dscale2 = _colsum(dh2 * nw2)
        dnw2 = dh2 * (1.0 + scale2)
        dpre2 = _colsum(dnw2 * n2)
        dz2_ref[...] = dz3 + _rms_bwd(dnw2 * pre2, n2, r2)

        @pl.when(i == 0)
        def _():
            stat_ref[...] = jnp.zeros_like(stat_ref)
            loss_ref[...] = jnp.zeros_like(loss_ref)

        for r, val in enumerate((dshift2, dscale2, dgate2, dpre2, dpost2)):
            stat_ref[r:r + 1, :] += val
        loss_ref[...] += part
        stat_ref[5:6, 0:128] += part

    lat = lambda i: (i, 0)
    fixed = lambda i: (0, 0)
    bf = lambda w: jax.ShapeDtypeStruct((rows, w), BF16)
    return pl.pallas_call(
        body, name=name, grid=(rows // TM,),
        out_shape=[jax.ShapeDtypeStruct((8, 128), F32), jax.ShapeDtypeStruct((rows, d), F32), bf(d), bf(dff), bf(dff),
                   bf(dff), bf(d), jax.ShapeDtypeStruct((8, d), F32)],
        in_specs=[pl.BlockSpec((TM, d), lat), pl.BlockSpec((8, d), fixed), pl.BlockSpec((8, d), fixed)]
        + [VMEM_SPEC] * 3 + [pl.BlockSpec((TM, d), lat)],
        out_specs=[pl.BlockSpec((8, 128), fixed), pl.BlockSpec((TM, d), lat), pl.BlockSpec((TM, d), lat),
                   pl.BlockSpec((TM, dff), lat), pl.BlockSpec((TM, dff), lat), pl.BlockSpec((TM, dff), lat),
                   pl.BlockSpec((TM, d), lat), pl.BlockSpec((8, d), fixed)],
        compiler_params=_cparams(dimension_semantics=("arbitrary",)),
    )(z2, modx, norms, w_gate, w_up, w_down, target)


def _mixer_tail_bwd(x_lat, p, o_list, dz2, y1, modx, norms, onorms, w_br_hg, w_br_gla, w_out, n_ctx_tiles, n_tiles,
                    name):
    rows, d = x_lat.shape
    total = n_tiles * TM

    def body(x_ref, ofw_hg, obw_hg, ofw_gla, obw_gla, p_hgate, p_ggate, p_ghg_a, p_ghg_b, p_ggla_a, p_ggla_b,
             dz2_ref, y1_ref, modx_ref, norm_ref, on_ref, wbh_ref, wbg_ref, wout_ref,
             dohg_ref, dogla_ref, dhgate_ref, dggate_ref, dghg_ref, dggla_ref, dy1_ref, dbhg_ref, dbgla_ref,
             stat_ref):
        i = pl.program_id(0)

        @pl.when(i == 0)
        def _():
            stat_ref[...] = jnp.zeros_like(stat_ref)

        @pl.when(i < n_ctx_tiles)
        def _():
            for ref in (dohg_ref, dogla_ref, dhgate_ref, dggate_ref, dghg_ref, dggla_ref):
                ref[...] = jnp.zeros_like(ref)

        @pl.when(i >= n_ctx_tiles)
        def _():
            post1, gate1 = norm_ref[1:2, :], modx_ref[2:3, :]
            hg_on, gla_on = on_ref[0:1, 0:HD], on_ref[1:2, 0:HD]
            p_gate_hg = jnp.concatenate([p_ghg_a[...], p_ghg_b[...]], axis=1)
            p_gate_gla = jnp.concatenate([p_ggla_a[...], p_ggla_b[...]], axis=1)
            ph, pg = p_hgate[...], p_ggate[...]
            t = _mixer_tail(x_ref[...], ofw_hg[...] + obw_hg[...], ofw_gla[...] + obw_gla[...], ph, pg,
                            p_gate_hg, p_gate_gla, hg_on, gla_on, wbh_ref[...], wbg_ref[...], wout_ref[...])
            dz2 = dz2_ref[...]
            n1, r1 = _rms(y1_ref[...])
            dgate1 = _colsum(dz2 * n1 * post1)
            tt = dz2 * gate1
            dpost1 = _colsum(tt * n1)
            dy1 = _rms_bwd(tt * post1, n1, r1).astype(BF16)
            dy1_ref[...] = dy1
            dmerged = _dot_nt(dy1, wout_ref[...])
            dghg_ref[...] = dmerged * t["b_hg"] * t["s_hg"] * (1.0 - t["s_hg"])
            dggla_ref[...] = dmerged * t["b_gla"] * t["s_gla"] * (1.0 - t["s_gla"])
            db_hg = (dmerged * t["s_hg"]).astype(BF16)
            db_gla = (dmerged * t["s_gla"]).astype(BF16)
            dbhg_ref[...] = db_hg
            dbgla_ref[...] = db_gla
            don_acc = []
            for (db, wb, pgate, on, ns, rs, gain, gate_ref, do_ref) in (
                    (db_hg, wbh_ref, ph, t["on_hg"], t["n_hg"], t["r_hg"], hg_on, dhgate_ref, dohg_ref),
                    (db_gla, wbg_ref, pg, t["on_gla"], t["n_gla"], t["r_gla"], gla_on, dggate_ref, dogla_ref)):
                dog = _dot_nt(db, wb[...])
                gate_ref[...] = dog * on * _dsilu(pgate)
                don = dog * _silu(pgate)
                acc = jnp.zeros((1, HD), F32)
                for h in range(NH):
                    sl = slice(h * HD, (h + 1) * HD)
                    acc = acc + _colsum(don[:, sl] * ns[h])
                    do_ref[:, sl] = _rms_bwd(don[:, sl] * gain, ns[h], rs[h])
                don_acc.append(acc)
            stat_ref[0:1, :] += dgate1
            stat_ref[1:2, :] += dpost1
            stat_ref[2:3, 0:HD] += don_acc[0]
            stat_ref[2:3, HD:2 * HD] += don_acc[1]

    lat = lambda i: (jnp.maximum(i - n_ctx_tiles, 0), 0)
    full = lambda i: (i, 0)
    fixed = lambda i: (0, 0)

    def pcol(blk):
        return pl.BlockSpec((TM, HW), lambda i: (i, blk))

    in_specs = ([pl.BlockSpec((TM, d), lat)] + [pl.BlockSpec((TM, HW), full)] * 4
                + [pcol(C_HGATE), pcol(C_GGATE), pcol(9), pcol(10), pcol(11), pcol(12)]
                + [pl.BlockSpec((TM, d), lat), pl.BlockSpec((TM, d), lat)]
                + [pl.BlockSpec((8, d), fixed)] * 3 + [VMEM_SPEC] * 3)
    f = lambda w: jax.ShapeDtypeStruct((total, w), F32)
    out_shape = [f(HW), f(HW), f(HW), f(HW), f(d), f(d), jax.ShapeDtypeStruct((rows, d), BF16),
                 jax.ShapeDtypeStruct((rows, d), BF16), jax.ShapeDtypeStruct((rows, d), BF16),
                 jax.ShapeDtypeStruct((8, d), F32)]
    out_specs = ([pl.BlockSpec((TM, HW), full)] * 4 + [pl.BlockSpec((TM, d), full)] * 2
                 + [pl.BlockSpec((TM, d), lat)] * 3 + [pl.BlockSpec((8, d), fixed)])
    return pl.pallas_call(
        body, name=name, grid=(n_tiles,), out_shape=out_shape, in_specs=in_specs, out_specs=out_specs,
        compiler_params=_cparams(dimension_semantics=("arbitrary",)),
    )(x_lat, *o_list, p, p, p, p, p, p, dz2, y1, modx, norms, onorms, w_br_hg, w_br_gla, w_out)


def _in_projection_bwd(ctx0, x0, dz2, modc, modx, pre1, w_t, pieces, n_ctx_tiles, name):
    d = x0.shape[1]
    rows = ctx0.shape[0] + x0.shape[0]
    lat_rows = dz2.shape[0]
    width = P_WIDTH
    n_pieces = len(pieces)

    def body(*refs):
        ctx_ref, x_ref, dz2_ref, modc_ref, modx_ref, pre_ref, w_ref = refs[:7]
        (dhq_f, dhq_b, dhi_f, dhi_b, dhf_f, dhf_b, dhgate, dgq_f, dgq_b, dgk_f, dgk_b, dgv_f, dgv_b, dggate,
         dghg, dggla, dlr_f, dlr_b) = refs[7:7 + n_pieces]
        dp_ref, gx_ref, stat_ref = refs[7 + n_pieces:]
        i = pl.program_id(0)
        is_ctx = i < n_ctx_tiles
        z = jnp.where(is_ctx, ctx_ref[...], x_ref[...])
        sections = [
            (0, dhq_f[...] + dhq_b[...]), (HW, dhi_f[...] + dhi_b[...]), (2 * HW, dhf_f[...]), (3 * HW, dhf_b[...]),
            (4 * HW, dhgate[...]), (5 * HW, dgq_f[...] + dgq_b[...]), (6 * HW, dgk_f[...] + dgk_b[...]),
            (7 * HW, dgv_f[...] + dgv_b[...]), (8 * HW, dggate[...]),
            (9 * HW, dghg[:, 0:HW]), (10 * HW, dghg[:, HW:2 * HW]),
            (11 * HW, dggla[:, 0:HW]), (12 * HW, dggla[:, HW:2 * HW]), (OFF_LR, dlr_f[...] + dlr_b[...])]
        dh = jnp.zeros((TM, d), F32)
        for off, val in sections:
            w = val.shape[1]
            vb = val.astype(BF16)
            dp_ref[:, off:off + w] = vb
            dh = dh + jnp.dot(vb, w_ref[_w_in_row(off):_w_in_row(off) + w, :], preferred_element_type=F32)
        n, r = _rms(z)
        pre = pre_ref[...]
        scale = jnp.where(is_ctx, modc_ref[1:2, :], modx_ref[1:2, :])
        nw = n * pre
        dshift = _colsum(dh)
        dscale = _colsum(dh * nw)
        dnw = dh * (1.0 + scale)
        dpre = _colsum(dnw * n)
        gx_ref[...] = dz2_ref[...] + _rms_bwd(dnw * pre, n, r)
        zero = jnp.zeros((1, d), F32)

        @pl.when(i == 0)
        def _():
            stat_ref[...] = jnp.zeros_like(stat_ref)

        stat_ref[0:1, :] += jnp.where(is_ctx, zero, dshift)
        stat_ref[1:2, :] += jnp.where(is_ctx, zero, dscale)
        stat_ref[2:3, :] += jnp.where(is_ctx, dshift, zero)
        stat_ref[3:4, :] += jnp.where(is_ctx, dscale, zero)
        stat_ref[4:5, :] += dpre

    full = lambda i: (i, 0)
    lat = lambda i: (jnp.maximum(i - n_ctx_tiles, 0), 0)
    fixed = lambda i: (0, 0)
    piece_specs = [pl.BlockSpec((TM, a.shape[1]), full) for a in pieces]
    in_specs = [pl.BlockSpec((TM, d), lambda i: (jnp.minimum(i, n_ctx_tiles - 1), 0)), pl.BlockSpec((TM, d), lat),
                pl.BlockSpec((TM, d), lat), pl.BlockSpec((8, d), fixed),
                pl.BlockSpec((8, d), fixed), pl.BlockSpec((1, d), fixed), VMEM_SPEC] + piece_specs
    return pl.pallas_call(
        body, name=name, grid=(rows // TM,),
        out_shape=[jax.ShapeDtypeStruct((rows, width), BF16), jax.ShapeDtypeStruct((lat_rows, d), F32),
                   jax.ShapeDtypeStruct((8, d), F32)],
        in_specs=in_specs,
        out_specs=[pl.BlockSpec((TM, width), full), pl.BlockSpec((TM, d), lat), pl.BlockSpec((8, d), fixed)],
        compiler_params=_cparams(dimension_semantics=("arbitrary",)),
    )(ctx0, x0, dz2, modc, modx, pre1, w_t, *pieces)


def _transposed_lhs_matmul(x_ref, dy_ref, o_ref, xt_ref):
    @pl.when(pl.program_id(1) == 0)
    def _():
        xt_ref[...] = x_ref[...].T

    o_ref[...] = jnp.dot(xt_ref[...], dy_ref[...], preferred_element_type=F32)


def _w_in_grad(dp, h1, n_cols, name):
    rows, d = h1.shape
    n_main = OFF_LR // HW
    lr0 = _w_in_row(OFF_LR)

    def body(x_ref, xlr_ref, h_ref, o_hbm, xt_ref, acc_ref, sem):
        i = pl.program_id(0)

        def main_copy(step):
            row = jnp.where(step < 9, step * HW, step * HW + 2 * RANK)
            return pltpu.make_async_copy(acc_ref, o_hbm.at[pl.ds(pl.multiple_of(row, 8), HW), :], sem)

        lr_copy = pltpu.make_async_copy(acc_ref.at[0:2 * RANK, :], o_hbm.at[lr0:lr0 + 2 * RANK, :], sem)

        @pl.when(i < n_main)
        def _():
            xt_ref[...] = x_ref[...].T

        @pl.when(i > 0)
        def _():
            main_copy(i - 1).wait()

        @pl.when(i < n_main)
        def _():
            acc_ref[...] = jnp.dot(xt_ref[...], h_ref[...], preferred_element_type=F32)
            main_copy(i).start()

        @pl.when(i == n_main)
        def _():
            xt_ref[0:128, :] = xlr_ref[...].T
            acc_ref[0:128, :] = jnp.dot(xt_ref[0:128, :], h_ref[...], preferred_element_type=F32)
            lr_copy.start()
            lr_copy.wait()

    return pl.pallas_call(
        body, name=name, grid=(n_main + 1,),
        out_shape=jax.ShapeDtypeStruct((n_cols, d), F32),
        in_specs=[pl.BlockSpec((rows, HW), lambda i: (0, jnp.minimum(i, n_main - 1))),
                  pl.BlockSpec((rows, 128), lambda i: (0, OFF_LR // 128)),
                  pl.BlockSpec((rows, d), lambda i: (0, 0))],
        out_specs=ANY_SPEC,
        scratch_shapes=[pltpu.VMEM((HW, rows), BF16), pltpu.VMEM((HW, d), F32), pltpu.SemaphoreType.DMA],
        compiler_params=_cparams(dimension_semantics=("arbitrary",)),
    )(dp, dp, h1)


def _weight_grad(xs, dy, name, tk=None, tn=512, k_first=0, k_tiles=None):
    rows = dy.shape[0]
    n = dy.shape[1]
    tn_ = min(tn, n)
    tk_ = xs.shape[1] if tk is None else tk
    k_tiles = xs.shape[1] // tk_ if k_tiles is None else k_tiles
    k = k_tiles * tk_

    return pl.pallas_call(
        functools.partial(_transposed_lhs_matmul), name=name, grid=(k_tiles, n // tn_),
        out_shape=jax.ShapeDtypeStruct((k, n), F32),
        in_specs=[pl.BlockSpec((rows, tk_), lambda i, j: (0, i + k_first)),
                  pl.BlockSpec((rows, tn_), lambda i, j: (0, j))],
        out_specs=pl.BlockSpec((tk_, tn_), lambda i, j: (i, j)),
        scratch_shapes=[pltpu.VMEM((tk_, rows), BF16)],
        compiler_params=_cparams(dimension_semantics=("parallel", "arbitrary")),
    )(xs, dy)


def _running_sum(x, fw):
    c = x.shape[0]
    row = lax.broadcasted_iota(jnp.int32, (c, 1), 0)
    s = 1
    while s < c:
        if fw:
            x = x + jnp.where(row >= s, pltpu.roll(x, s, axis=0), 0.0)
        else:
            x = x + jnp.where(row < c - s, pltpu.roll(x, c - s, axis=0), 0.0)
        s *= 2
    return x


def _chunk_terms(q, k, g, fw):
    c = CHUNK
    r = lax.broadcasted_iota(jnp.int32, (c, c), 0)
    s = lax.broadcasted_iota(jnp.int32, (c, c), 1)
    causal = (s <= r) if fw else (s >= r)
    causal_t = (s >= r) if fw else (s <= r)
    cum = _running_sum(g, fw)
    row = lax.broadcasted_iota(jnp.int32, (c, 1), 0)
    pos = row if fw else (c - 1 - row)
    starts = [None]
    for j in range(1, NSUB):
        rj = SUB * j - 1 if fw else c - SUB * j
        starts.append(cum[rj:rj + 1, :])
    in_blk = [(pos >= SUB * j) & (pos < SUB * (j + 1)) for j in range(NSUB)]
    e = [jnp.exp(cum)]
    for j in range(1, NSUB):
        e.append(jnp.exp(jnp.where(pos >= SUB * j, cum - starts[j], -1e30)))
    own = jnp.zeros_like(cum)
    for j in range(1, NSUB):
        own = own + jnp.where(in_blk[j], starts[j], 0.0)
    kscale = jnp.exp(own - cum)
    rend = c - 1 if fw else 0
    cend = cum[rend:rend + 1, :]
    tail = jnp.exp(cend - cum)
    qcat = jnp.concatenate([q * e[j] for j in range(NSUB)], axis=1).astype(BF16)
    kt = k * kscale
    km = jnp.concatenate([jnp.where(in_blk[j], kt, 0.0) for j in range(NSUB)], axis=1).astype(BF16)
    return dict(causal=causal, causal_t=causal_t, e=e, in_blk=in_blk, kscale=kscale, cend=cend, tail=tail,
                qcat=qcat, km=km)


def _chunk_fwd(q, k, v, g, st0, fw):
    t = _chunk_terms(q, k, g, fw)
    a = jnp.where(t["causal"], _dot_nt(t["qcat"], t["km"]), 0.0)
    o = _dot(a, v) + _dot_nt(t["qcat"][:, 0:HD], st0)
    st1 = st0 * jnp.exp(t["cend"]) + _dot_tn(v, k * t["tail"])
    return o, st1


def _chunk_bwd(q, k, v, g, st0, do, dst1, fw):
    t = _chunk_terms(q, k, g, fw)
    qcat, km, e = t["qcat"], t["km"], t["e"]
    a_t = jnp.where(t["causal_t"], _dot_nt(km, qcat), 0.0)
    ktail = k * t["tail"]
    dv = _dot(a_t, do) + _dot_nt(ktail, dst1)
    da = jnp.where(t["causal"], _dot_nt(do, v), 0.0)
    da_t = jnp.where(t["causal_t"], _dot_nt(v, do), 0.0)
    dqcat = _dot(da, km)
    dq_inter = e[0] * _dot(do, st0)
    dq = dq_inter
    for j in range(NSUB):
        dq = dq + e[j] * dqcat[:, j * HD:(j + 1) * HD]
    dkm = _dot(da_t, qcat)
    dkt = jnp.zeros_like(k)
    for j in range(NSUB):
        dkt = dkt + jnp.where(t["in_blk"][j], dkm[:, j * HD:(j + 1) * HD], 0.0)
    dk_inter = _dot(v, dst1) * t["tail"]
    dk = dkt * t["kscale"] + dk_inter
    dcum = q * dq_inter - k * dk_inter
    for j in range(NSUB):
        sl = slice(j * HD, (j + 1) * HD)
        dcum = dcum + qcat[:, sl].astype(F32) * dqcat[:, sl] - km[:, sl].astype(F32) * dkm[:, sl]
    ecend = jnp.exp(t["cend"])
    end = ecend * _colsum(st0 * dst1) + _colsum(k * dk_inter)
    dg = _running_sum(dcum, not fw) + end
    dst0 = dst1 * ecend + _dot_tn(do, q * e[0])
    return dq, dk, dv, dg, dst0


def _running_sums(xs, fws):
    c = xs[0].shape[0]
    row = lax.broadcasted_iota(jnp.int32, (c, 1), 0)
    s = 1
    while s < c:
        xs = [x + (jnp.where(row >= s, pltpu.roll(x, s, axis=0), 0.0) if fw else
                   jnp.where(row < c - s, pltpu.roll(x, c - s, axis=0), 0.0)) for x, fw in zip(xs, fws)]
        s *= 2
    return xs


def _chunks_terms(qs, ks, gs, fws):
    c = CHUNK
    n = len(qs)
    r = lax.broadcasted_iota(jnp.int32, (c, c), 0)
    s = lax.broadcasted_iota(jnp.int32, (c, c), 1)
    row = lax.broadcasted_iota(jnp.int32, (c, 1), 0)
    per_dir = {}
    for fw in set(fws):
        pos = row if fw else (c - 1 - row)
        per_dir[fw] = dict(
            causal=(s <= r) if fw else (s >= r), causal_t=(s >= r) if fw else (s <= r), pos=pos,
            in_blk=[(pos >= SUB * j) & (pos < SUB * (j + 1)) for j in range(NSUB)],
            start_row=[None] + [SUB * j - 1 if fw else c - SUB * j for j in range(1, NSUB)],
            rend=c - 1 if fw else 0)
    dirs = [per_dir[fw] for fw in fws]
    cums = _running_sums(gs, fws)
    starts = [[None] + [cum[d["start_row"][j]:d["start_row"][j] + 1, :] for j in range(1, NSUB)]
              for cum, d in zip(cums, dirs)]
    es = [[jnp.exp(cum) for cum in cums]]
    for j in range(1, NSUB):
        es.append([jnp.exp(jnp.where(d["pos"] >= SUB * j, cum - st[j], -1e30)) for cum, st, d in zip(cums, starts, dirs)])
    owns = [sum(jnp.where(d["in_blk"][j], st[j], 0.0) for j in range(1, NSUB)) for st, d in zip(starts, dirs)]
    kscales = [jnp.exp(own - cum) for own, cum in zip(owns, cums)]
    cends = [cum[d["rend"]:d["rend"] + 1, :] for cum, d in zip(cums, dirs)]
    tails = [jnp.exp(cend - cum) for cend, cum in zip(cends, cums)]
    qcats = [jnp.concatenate([q * es[j][i] for j in range(NSUB)], axis=1).astype(BF16) for i, q in enumerate(qs)]
    kts = [k * ksc for k, ksc in zip(ks, kscales)]
    kms = [jnp.concatenate([jnp.where(d["in_blk"][j], kt, 0.0) for j in range(NSUB)], axis=1).astype(BF16)
           for kt, d in zip(kts, dirs)]
    e_by_lane = [[es[j][i] for j in range(NSUB)] for i in range(n)]
    return dict(dirs=dirs, e=e_by_lane, kscale=kscales, cend=cends, tail=tails, qcat=qcats, km=kms)


def _chunks_fwd(qs, ks, vs, gs, st0s, fws):
    t = _chunks_terms(qs, ks, gs, fws)
    scores = [_dot_nt(qc, km) for qc, km in zip(t["qcat"], t["km"])]
    a = [jnp.where(d["causal"], sc, 0.0) for sc, d in zip(scores, t["dirs"])]
    inter = [_dot_nt(qc[:, 0:HD], st0) for qc, st0 in zip(t["qcat"], st0s)]
    intra = [_dot(a_, v) for a_, v in zip(a, vs)]
    os_ = [x + y for x, y in zip(intra, inter)]
    upd = [_dot_tn(v, k * tl) for v, k, tl in zip(vs, ks, t["tail"])]
    st1s = [st0 * jnp.exp(ce) + u for st0, ce, u in zip(st0s, t["cend"], upd)]
    return os_, st1s


def _chunks_bwd(qs, ks, vs, gs, st0s, dos, dst1s, fws):
    n = len(qs)
    t = _chunks_terms(qs, ks, gs, fws)
    qcat, km, e, dirs = t["qcat"], t["km"], t["e"], t["dirs"]
    a_t = [jnp.where(d["causal_t"], _dot_nt(km_, qc), 0.0) for km_, qc, d in zip(km, qcat, dirs)]
    ktail = [k * tl for k, tl in zip(ks, t["tail"])]
    dv_a = [_dot(at, do) for at, do in zip(a_t, dos)]
    dv_b = [_dot_nt(kt, ds) for kt, ds in zip(ktail, dst1s)]
    dv = [x + y for x, y in zip(dv_a, dv_b)]
    da = [jnp.where(d["causal"], _dot_nt(do, v), 0.0) for do, v, d in zip(dos, vs, dirs)]
    da_t = [jnp.where(d["causal_t"], _dot_nt(v, do), 0.0) for do, v, d in zip(dos, vs, dirs)]
    dqcat = [_dot(da_, km_) for da_, km_ in zip(da, km)]
    dq_inter = [e[i][0] * _dot(dos[i], st0s[i]) for i in range(n)]
    dkm = [_dot(dat, qc) for dat, qc in zip(da_t, qcat)]
    dk_inter = [_dot(v, ds) * tl for v, ds, tl in zip(vs, dst1s, t["tail"])]
    dq = [dq_inter[i] + sum(e[i][j] * dqcat[i][:, j * HD:(j + 1) * HD] for j in range(NSUB)) for i in range(n)]
    dkt = [sum(jnp.where(dirs[i]["in_blk"][j], dkm[i][:, j * HD:(j + 1) * HD], 0.0) for j in range(NSUB))
           for i in range(n)]
    dk = [dkt[i] * t["kscale"][i] + dk_inter[i] for i in range(n)]
    dcum = [qs[i] * dq_inter[i] - ks[i] * dk_inter[i]
            + sum(qcat[i][:, j * HD:(j + 1) * HD].astype(F32) * dqcat[i][:, j * HD:(j + 1) * HD]
                  - km[i][:, j * HD:(j + 1) * HD].astype(F32) * dkm[i][:, j * HD:(j + 1) * HD] for j in range(NSUB))
            for i in range(n)]
    ecend = [jnp.exp(ce) for ce in t["cend"]]
    end = [ecend[i] * _colsum(st0s[i] * dst1s[i]) + _colsum(ks[i] * dk_inter[i]) for i in range(n)]
    sums = _running_sums(dcum, [not fw for fw in fws])
    dg = [sm + en for sm, en in zip(sums, end)]
    upd = [_dot_tn(dos[i], qs[i] * e[i][0]) for i in range(n)]
    dst0 = [dst1s[i] * ecend[i] + upd[i] for i in range(n)]
    return dq, dk, dv, dg, dst0


def _chunk_index(step, n_ctx_chunks, n_chunks, fw):
    if fw:
        return step
    return jnp.where(step < n_ctx_chunks, n_ctx_chunks - 1 - step, n_chunks - 1 + n_ctx_chunks - step)


def _hg_inputs(hq, hf, lbv, d_idx, sl):
    lb = _sigmoid(lbv[d_idx:d_idx + 1, sl] - lbv[2 + d_idx:3 + d_idx, sl])
    sg = _sigmoid(hf)
    f = lb + (1.0 - lb) * sg
    return _silu(hq), 1.0 - f, jnp.log(f), f, sg, lb


def _scan_fwd(p, side, n_ctx_chunks, fw, branch, name):
    rows = p.shape[0]
    n_chunks = rows // CHUNK
    d_idx = 0 if fw else 1
    hg = branch == "hg"
    cols = (C_HQ, C_HI, C_HF_FW + d_idx) if hg else (C_GQ, C_GK, C_GV)

    def body(*refs):
        if hg:
            a_ref, b_ref, c_ref, lb_ref, o_ref, st_ref, state = refs
        else:
            a_ref, b_ref, c_ref, lr_ref, wgk_ref, bgk_ref, o_ref, st_ref, state = refs
            logits = _dot(lr_ref[...], wgk_ref[...]) + bgk_ref[...]
            g_all = _log_sigmoid(logits) * (1.0 / GATE_NORM)

        @pl.when(pl.program_id(0) == 0)
        def _():
            state[...] = jnp.zeros_like(state)

        for h in range(NH):
            sl = slice(h * HD, (h + 1) * HD)
            if hg:
                q, k, g, _, _, _ = _hg_inputs(a_ref[:, sl], c_ref[:, sl], lb_ref[...], d_idx, sl)
                v = b_ref[:, sl]
            else:
                q, k, v, g = a_ref[:, sl] * (HD ** -0.5), b_ref[:, sl], c_ref[:, sl], g_all[:, sl]
            st0 = state[h]
            st_ref[0, h] = st0
            o, st1 = _chunk_fwd(q, k, v, g, st0, fw)
            o_ref[:, sl] = o
            state[h] = st1

    def cmap(blk):
        return pl.BlockSpec((CHUNK, HW), lambda j: (_chunk_index(j, n_ctx_chunks, n_chunks, fw), blk))

    fixed = lambda j: (0, 0)
    in_specs = [cmap(cols[0]), cmap(cols[1]), cmap(cols[2])]
    if hg:
        in_specs += [pl.BlockSpec((4, HW), fixed)]
        args = (p, p, p, side)
    else:
        in_specs += [pl.BlockSpec((CHUNK, 128), lambda j: (_chunk_index(j, n_ctx_chunks, n_chunks, fw), OFF_LR // 128)),
                     pl.BlockSpec((128, HW), fixed), pl.BlockSpec((1, HW), fixed)]
        args = (p, p, p, p, side[0], side[1])
    return pl.pallas_call(
        body, name=name, grid=(n_chunks,),
        out_shape=[jax.ShapeDtypeStruct((rows, HW), F32), jax.ShapeDtypeStruct((n_chunks, NH, HD, HD), F32)],
        in_specs=in_specs,
        out_specs=[pl.BlockSpec((CHUNK, HW), lambda j: (_chunk_index(j, n_ctx_chunks, n_chunks, fw), 0)),
                   pl.BlockSpec((1, NH, HD, HD), lambda j: (_chunk_index(j, n_ctx_chunks, n_chunks, fw), 0, 0, 0))],
        scratch_shapes=[pltpu.VMEM((NH, HD, HD), F32)],
        compiler_params=_cparams(dimension_semantics=("arbitrary",)),
    )(*args)


def _scan_fwd_both(p, side, n_ctx_chunks, branch, name):
    rows = p.shape[0]
    n_chunks = rows // CHUNK
    hg = branch == "hg"
    n_in = 4 if hg else 6

    def body(*refs):
        ins, outs, state = refs[:2 * n_in], refs[2 * n_in:2 * n_in + 4], refs[-1]

        @pl.when(pl.program_id(0) == 0)
        def _():
            state[...] = jnp.zeros_like(state)

        lanes, where = [], []
        for di, fw in enumerate((True, False)):
            r = ins[di * n_in:(di + 1) * n_in]
            o_ref, st_ref = outs[2 * di], outs[2 * di + 1]
            if hg:
                a_ref, b_ref, c_ref, lb_ref = r
            else:
                a_ref, b_ref, c_ref, lr_ref, wgk_ref, bgk_ref = r
                logits = _dot(lr_ref[...], wgk_ref[...]) + bgk_ref[...]
                g_all = _log_sigmoid(logits) * (1.0 / GATE_NORM)
            for h in range(NH):
                sl = slice(h * HD, (h + 1) * HD)
                if hg:
                    q, k, g, _, _, _ = _hg_inputs(a_ref[:, sl], c_ref[:, sl], lb_ref[...], di, sl)
                    v = b_ref[:, sl]
                else:
                    q, k, v, g = a_ref[:, sl] * (HD ** -0.5), b_ref[:, sl], c_ref[:, sl], g_all[:, sl]
                lanes.append((q, k, v, g, state[di, h], fw))
                where.append((di, h, sl, o_ref, st_ref))
        qs, ks, vs, gs, st0s, fws = (list(col) for col in zip(*lanes))
        os_, st1s = _chunks_fwd(qs, ks, vs, gs, st0s, fws)
        for (di, h, sl, o_ref, st_ref), st0, o, st1 in zip(where, st0s, os_, st1s):
            st_ref[0, h] = st0
            o_ref[:, sl] = o
            state[di, h] = st1

    fixed = lambda j: (0, 0)
    in_specs, args, out_specs = [], [], []
    for di, fw in enumerate((True, False)):
        chunk = functools.partial(_chunk_index, n_ctx_chunks=n_ctx_chunks, n_chunks=n_chunks, fw=fw)

        def cmap(blk, width=HW, chunk=chunk):
            return pl.BlockSpec((CHUNK, width), lambda j: (chunk(j), blk))

        if hg:
            in_specs += [cmap(C_HQ), cmap(C_HI), cmap(C_HF_FW + di), pl.BlockSpec((4, HW), fixed)]
            args += [p, p, p, side]
        else:
            in_specs += [cmap(C_GQ), cmap(C_GK), cmap(C_GV), cmap(OFF_LR // 128, 128),
                         pl.BlockSpec((128, HW), fixed), pl.BlockSpec((1, HW), fixed)]
            args += [p, p, p, p, side[di][0], side[di][1]]
        out_specs += [cmap(0), pl.BlockSpec((1, NH, HD, HD), lambda j, chunk=chunk: (chunk(j), 0, 0, 0))]
    return pl.pallas_call(
        body, name=name, grid=(n_chunks,),
        out_shape=[jax.ShapeDtypeStruct((rows, HW), F32), jax.ShapeDtypeStruct((n_chunks, NH, HD, HD), F32)] * 2,
        in_specs=in_specs, out_specs=out_specs,
        scratch_shapes=[pltpu.VMEM((2, NH, HD, HD), F32)],
        compiler_params=_cparams(dimension_semantics=("arbitrary",)),
    )(*args)


def _scan_bwd_both(p, side, states, d_o, n_ctx_chunks, branch, name):
    rows = p.shape[0]
    n_chunks = rows // CHUNK
    hg = branch == "hg"
    n_in = 6 if hg else 8
    n_out = 4 if hg else 6

    def body(*refs):
        ins, outs, dstate = refs[:2 * n_in], refs[2 * n_in:2 * n_in + 2 * n_out], refs[-1]
        first = pl.program_id(0) == 0

        @pl.when(first)
        def _():
            dstate[...] = jnp.zeros_like(dstate)

        lanes, where, extra, ctx = [], [], [], []
        for di, fw in enumerate((True, False)):
            r, w = ins[di * n_in:(di + 1) * n_in], outs[di * n_out:(di + 1) * n_out]
            if hg:
                a_ref, b_ref, c_ref, lb_ref, st_ref, do_ref = r
                da_ref, db_ref, dc_ref, dlb_ref = w
                acc_refs = (dlb_ref,)
            else:
                a_ref, b_ref, c_ref, lr_ref, wgk_ref, bgk_ref, st_ref, do_ref = r
                da_ref, db_ref, dc_ref, dlr_ref, dwgk_ref, dbias_ref = w
                acc_refs = (dwgk_ref, dbias_ref)
                lr = lr_ref[...]
                logits = _dot(lr, wgk_ref[...]) + bgk_ref[...]
                g_all = _log_sigmoid(logits) * (1.0 / GATE_NORM)

            @pl.when(first)
            def _(acc_refs=acc_refs):
                for ref in acc_refs:
                    ref[...] = jnp.zeros_like(ref)

            for h in range(NH):
                sl = slice(h * HD, (h + 1) * HD)
                if hg:
                    hq, hf = a_ref[:, sl], c_ref[:, sl]
                    q, k, g, f, sg, lb = _hg_inputs(hq, hf, lb_ref[...], di, sl)
                    v = b_ref[:, sl]
                    extra.append((hq, f, sg, lb))
                else:
                    q, k, v, g = a_ref[:, sl] * (HD ** -0.5), b_ref[:, sl], c_ref[:, sl], g_all[:, sl]
                    extra.append(None)
                lanes.append((q, k, v, g, st_ref[0, h], do_ref[:, sl], dstate[di, h], fw))
                where.append((di, h, sl))
            ctx.append((w, None if hg else (lr, logits, wgk_ref)))

        qs, ks, vs, gs, st0s, dos, dst1s, fws = (list(col) for col in zip(*lanes))
        dqs, dks, dvs, dgs, dst0s = _chunks_bwd(qs, ks, vs, gs, st0s, dos, dst1s, fws)
        dg_parts = {0: [], 1: []}
        for (di, h, sl), ex, dq, dk, dv, dg, dst0 in zip(where, extra, dqs, dks, dvs, dgs, dst0s):
            dstate[di, h] = dst0
            w = ctx[di][0]
            if hg:
                hq, f, sg, lb = ex
                da_ref, db_ref, dc_ref, dlb_ref = w
                da_ref[:, sl] = dq * _dsilu(hq)
                db_ref[:, sl] = dv
                df = dg / f - dk
                dc_ref[:, sl] = df * (1.0 - lb) * sg * (1.0 - sg)
                dlb_ref[0:1, sl] += _colsum(df * (1.0 - sg))
            else:
                da_ref, db_ref, dc_ref = w[:3]
                da_ref[:, sl] = dq * (HD ** -0.5)
                db_ref[:, sl] = dk
                dc_ref[:, sl] = dv
                dg_parts[di].append(dg)
        if not hg:
            for di in range(2):
                dlr_ref, dwgk_ref, dbias_ref = ctx[di][0][3:]
                lr, logits, wgk_ref = ctx[di][1]
                dlogits = jnp.concatenate(dg_parts[di], axis=1) * (1.0 / GATE_NORM) * (1.0 - _sigmoid(logits))
                dlr_ref[...] = _dot_nt(dlogits, wgk_ref[...])
                dwgk_ref[...] += _dot_tn(lr, dlogits)
                dbias_ref[0:1, :] += _colsum(dlogits)

    fixed = lambda j: (0, 0)
    big = jax.ShapeDtypeStruct((rows, HW), F32)
    in_specs, args, out_shape, out_specs = [], [], [], []
    for di, fw in enumerate((True, False)):
        def chunk_of(j, fw=fw):
            return _chunk_index(n_chunks - 1 - j, n_ctx_chunks, n_chunks, fw)

        def cmap(blk, width=HW, chunk_of=chunk_of):
            return pl.BlockSpec((CHUNK, width), lambda j: (chunk_of(j), blk))

        st_spec = pl.BlockSpec((1, NH, HD, HD), lambda j, chunk_of=chunk_of: (chunk_of(j), 0, 0, 0))
        if hg:
            in_specs += [cmap(C_HQ), cmap(C_HI), cmap(C_HF_FW + di), pl.BlockSpec((4, HW), fixed), st_spec, cmap(0)]
            args += [p, p, p, side, states[di], d_o]
            out_shape += [big, big, big, jax.ShapeDtypeStruct((8, HW), F32)]
            out_specs += [cmap(0), cmap(0), cmap(0), pl.BlockSpec((8, HW), fixed)]
        else:
            in_specs += [cmap(C_GQ), cmap(C_GK), cmap(C_GV), cmap(OFF_LR // 128, 128),
                         pl.BlockSpec((128, HW), fixed), pl.BlockSpec((1, HW), fixed), st_spec, cmap(0)]
            args += [p, p, p, p, side[di][0], side[di][1], states[di], d_o]
            out_shape += [big, big, big, jax.ShapeDtypeStruct((rows, 128), F32),
                          jax.ShapeDtypeStruct((128, HW), F32), jax.ShapeDtypeStruct((8, HW), F32)]
            out_specs += [cmap(0), cmap(0), cmap(0), cmap(0, 128), pl.BlockSpec((128, HW), fixed),
                          pl.BlockSpec((8, HW), fixed)]
    return pl.pallas_call(
        body, name=name, grid=(n_chunks,), out_shape=out_shape, in_specs=in_specs, out_specs=out_specs,
        scratch_shapes=[pltpu.VMEM((2, NH, HD, HD), F32)],
        compiler_params=_cparams(dimension_semantics=("arbitrary",)),
    )(*args)


def _scan_bwd(p, side, states, d_o, n_ctx_chunks, fw, branch, name):
    rows = p.shape[0]
    n_chunks = rows // CHUNK
    d_idx = 0 if fw else 1
    hg = branch == "hg"
    cols = (C_HQ, C_HI, C_HF_FW + d_idx) if hg else (C_GQ, C_GK, C_GV)

    def body(*refs):
        if hg:
            a_ref, b_ref, c_ref, lb_ref, st_ref, do_ref, da_ref, db_ref, dc_ref, dlb_ref, dstate = refs
        else:
            (a_ref, b_ref, c_ref, lr_ref, wgk_ref, bgk_ref, st_ref, do_ref, da_ref, db_ref, dc_ref, dlr_ref,
             dwgk_ref, dbias_ref, dstate) = refs
            lr = lr_ref[...]
            logits = _dot(lr, wgk_ref[...]) + bgk_ref[...]
            g_all = _log_sigmoid(logits) * (1.0 / GATE_NORM)

        @pl.when(pl.program_id(0) == 0)
        def _():
            dstate[...] = jnp.zeros_like(dstate)
            if hg:
                dlb_ref[...] = jnp.zeros_like(dlb_ref)
            else:
                dwgk_ref[...] = jnp.zeros_like(dwgk_ref)
                dbias_ref[...] = jnp.zeros_like(dbias_ref)

        dg_parts = []
        for h in range(NH):
            sl = slice(h * HD, (h + 1) * HD)
            if hg:
                hq, hf = a_ref[:, sl], c_ref[:, sl]
                q, k, g, f, sg, lb = _hg_inputs(hq, hf, lb_ref[...], d_idx, sl)
                v = b_ref[:, sl]
            else:
                q, k, v, g = a_ref[:, sl] * (HD ** -0.5), b_ref[:, sl], c_ref[:, sl], g_all[:, sl]
            dq, dk, dv, dg, dst0 = _chunk_bwd(q, k, v, g, st_ref[0, h], do_ref[:, sl], dstate[h], fw)
            dstate[h] = dst0
            if hg:
                da_ref[:, sl] = dq * _dsilu(hq)
                db_ref[:, sl] = dv
                df = dg / f - dk
                dc_ref[:, sl] = df * (1.0 - lb) * sg * (1.0 - sg)
                dlb_ref[0:1, sl] += _colsum(df * (1.0 - sg))
            else:
                da_ref[:, sl] = dq * (HD ** -0.5)
                db_ref[:, sl] = dk
                dc_ref[:, sl] = dv
                dg_parts.append(dg)
        if not hg:
            dlogits = jnp.concatenate(dg_parts, axis=1) * (1.0 / GATE_NORM) * (1.0 - _sigmoid(logits))
            dlr_ref[...] = _dot_nt(dlogits, wgk_ref[...])
            dwgk_ref[...] += _dot_tn(lr, dlogits)
            dbias_ref[0:1, :] += _colsum(dlogits)

    def chunk_of(j):
        return _chunk_index(n_chunks - 1 - j, n_ctx_chunks, n_chunks, fw)

    def cmap(blk, width=HW):
        return pl.BlockSpec((CHUNK, width), lambda j: (chunk_of(j), blk))

    fixed = lambda j: (0, 0)
    st_spec = pl.BlockSpec((1, NH, HD, HD), lambda j: (chunk_of(j), 0, 0, 0))
    big = jax.ShapeDtypeStruct((rows, HW), F32)
    if hg:
        in_specs = [cmap(cols[0]), cmap(cols[1]), cmap(cols[2]), pl.BlockSpec((4, HW), fixed), st_spec, cmap(0)]
        args = (p, p, p, side, states, d_o)
        out_shape = [big, big, big, jax.ShapeDtypeStruct((8, HW), F32)]
        out_specs = [cmap(0), cmap(0), cmap(0), pl.BlockSpec((8, HW), fixed)]
    else:
        in_specs = [cmap(cols[0]), cmap(cols[1]), cmap(cols[2]), cmap(OFF_LR // 128, 128),
                    pl.BlockSpec((128, HW), fixed), pl.BlockSpec((1, HW), fixed), st_spec, cmap(0)]
        args = (p, p, p, p, side[0], side[1], states, d_o)
        out_shape = [big, big, big, jax.ShapeDtypeStruct((rows, 128), F32), jax.ShapeDtypeStruct((128, HW), F32),
                     jax.ShapeDtypeStruct((8, HW), F32)]
        out_specs = [cmap(0), cmap(0), cmap(0), cmap(0, 128), pl.BlockSpec((128, HW), fixed),
                     pl.BlockSpec((8, HW), fixed)]
    return pl.pallas_call(
        body, name=name, grid=(n_chunks,), out_shape=out_shape, in_specs=in_specs, out_specs=out_specs,
        scratch_shapes=[pltpu.VMEM((NH, HD, HD), F32)],
        compiler_params=_cparams(dimension_semantics=("arbitrary",)),
    )(*args)


SMALL_ROWS = 56
ROWS_MOD_X = (0, 1, 8, 16, 17, 18)
ROWS_MOD_C = (2, 3)
ROW_PRE1, ROW_POST1, ROW_ONORM, ROW_PRE2, ROW_POST2, ROW_LB, ROW_BGK, ROW_WGK = 4, 9, 10, 19, 20, 24, 32, 40
ROW_LOSS = 21


def _reduce_small(gathered, lb_full, name):
    _, _, d = gathered.shape

    def body(g_ref, lb_ref, sum_ref, dmod_ref, dbmod_ref, dlb_ref):
        total = g_ref[0]
        for b in range(1, N_DEV):
            total = total + g_ref[b]
        sum_ref[...] = total
        dmod_ref[...] = jnp.zeros_like(dmod_ref)
        for m in range(N_MOD):
            col = slice(m * d, (m + 1) * d)
            acc = jnp.zeros((1, d), F32)
            for b in range(N_DEV):
                row = g_ref[b, ROWS_MOD_X[m]:ROWS_MOD_X[m] + 1, :]
                dmod_ref[b:b + 1, col] = row
                acc = acc + row
            if m < 2:
                ctx_row = total[ROWS_MOD_C[m]:ROWS_MOD_C[m] + 1, :]
                dmod_ref[8:9, col] = ctx_row
                acc = acc + ctx_row
            dbmod_ref[:, col] = acc
        lbv = lb_ref[...]
        for dd in range(2):
            lb = _sigmoid(lbv[dd:dd + 1, :] - lbv[2 + dd:3 + dd, :])
            gl = total[ROW_LB:ROW_LB + 1, dd * HW:(dd + 1) * HW] * lb * (1.0 - lb)
            dlb_ref[dd:dd + 1, :] = gl
            dlb_ref[2 + dd:3 + dd, :] = -gl

    return pl.pallas_call(
        body, name=name,
        out_shape=[jax.ShapeDtypeStruct((SMALL_ROWS, d), F32), jax.ShapeDtypeStruct((16, N_MOD * d), F32),
                   jax.ShapeDtypeStruct((1, N_MOD * d), F32), jax.ShapeDtypeStruct((4, HW), F32)],
        in_specs=[VMEM_SPEC] * 2, out_specs=[VMEM_SPEC] * 4, compiler_params=_cparams(),
    )(gathered, lb_full)


def _c_ctx_grad(gathered, c_ctx_row, name):
    def body(g_ref, c_ref, o_ref):
        acc = g_ref[0, 0:1, :]
        for chip in range(1, N_CHIP):
            acc = acc + g_ref[2 * chip, 0:1, :]
        o_ref[...] = acc * _dsilu(c_ref[...])

    return pl.pallas_call(
        body, name=name, out_shape=jax.ShapeDtypeStruct(c_ctx_row.shape, F32),
        in_specs=[VMEM_SPEC] * 2, out_specs=VMEM_SPEC, compiler_params=_cparams(),
    )(gathered, c_ctx_row)


def _relayout_w_in(w):
    pad = jnp.zeros((w.shape[0], 128 - 2 * RANK), w.dtype)
    return jnp.concatenate([w[:, :9 * HW], w[:, 9 * HW + 2 * RANK:], w[:, 9 * HW:9 * HW + 2 * RANK], pad], axis=1)


def _relayout_w_in_rows(wt):
    pad = jnp.zeros((128 - 2 * RANK, wt.shape[1]), wt.dtype)
    return jnp.concatenate([wt[:9 * HW], wt[9 * HW + 2 * RANK:], wt[9 * HW:9 * HW + 2 * RANK], pad], axis=0)


def _w_in_grad_rows(g_main, g_lr):
    return jnp.concatenate([g_main[:9 * HW], g_lr[:2 * RANK], g_main[9 * HW:]], axis=0)


def _w_in_grad_blocks(g_main, g_lr, n_blocks):
    lr0 = 9 * HW
    n = (g_main.shape[1] + 2 * RANK) // n_blocks

    def cols(lo, hi):
        out = []
        if lo < lr0:
            out.append(g_main[:, lo:min(hi, lr0)])
        if hi > lr0 and lo < lr0 + 2 * RANK:
            out.append(g_lr[:, max(lo, lr0) - lr0:min(hi, lr0 + 2 * RANK) - lr0])
        if hi > lr0 + 2 * RANK:
            out.append(g_main[:, max(lo, lr0 + 2 * RANK) - 2 * RANK:hi - 2 * RANK])
        return out

    return jnp.stack([jnp.concatenate(cols(j * n, (j + 1) * n), axis=1) for j in range(n_blocks)])


def _blocked(full, n_blocks):
    k, n = full.shape
    return full.reshape(k, n_blocks, n // n_blocks).transpose(1, 0, 2)


def _unblocked(blocks):
    nb, k, n = blocks.shape
    return blocks.transpose(1, 0, 2).reshape(k, nb * n)


def _sample_front(x0, ctx0, modc, modx, norm_pre1, lb_full, gla_side, w_in_r):
    ctx_len = ctx0.shape[0]
    n_ctx_tiles = ctx_len // TM
    n_ctx_chunks = ctx_len // CHUNK
    h1, p = _in_projection(ctx0, x0, modc, modx, norm_pre1, w_in_r, n_ctx_tiles, "in_projection")
    o_hg_fw, st_hg_fw, o_hg_bw, st_hg_bw = _scan_fwd_both(p, lb_full, n_ctx_chunks, "hg", "scan_hg")
    o_gla_fw, st_gla_fw, o_gla_bw, st_gla_bw = _scan_fwd_both(p, gla_side, n_ctx_chunks, "gla", "scan_gla")
    return dict(h1=h1, p=p, o_list=[o_hg_fw, o_hg_bw, o_gla_fw, o_gla_bw],
                states=[st_hg_fw, st_hg_bw, st_gla_fw, st_gla_bw])


def _sample_back(reduce, front, x0, ctx0, target0, modc, modx, norm_pre1, norms, onorms, lb_full, gla_side, w_in_r,
                 wbh, wbg, wout, wg, wu, wd):
    seq, d = x0.shape
    ctx_len = ctx0.shape[0]
    n_ctx_tiles = ctx_len // TM
    n_tiles = (ctx_len + seq) // TM
    n_ctx_chunks = ctx_len // CHUNK
    h1, p, o_list = front["h1"], front["p"], front["o_list"]
    st_hg_fw, st_hg_bw, st_gla_fw, st_gla_bw = front["states"]
    z2, y1, merged, og_hg, og_gla = _mixer_tail_fwd(x0, p, o_list, modx, norms, onorms, wbh, wbg, wout, n_ctx_tiles,
                                                    "mixer_tail")
    loss_part, dz2, h2, a_act, du, dv, dy2, stat_ffn = _ffn_fwd_bwd(z2, modx, norms, wg, wu, wd, target0, "ffn")
    dff = wg.shape[0]
    tok = reduce("ffn", [_weight_grad(du, h2, "grad_w_ff_gate", tk=dff // 2, tn=d),
                         _weight_grad(dv, h2, "grad_w_ff_up", tk=dff // 2, tn=d),
                         _weight_grad(a_act, dy2, "grad_w_ff_down", tk=dff // 2)])

    (d_ohg, d_ogla, d_hgate, d_ggate, d_ghg, d_ggla, dy1, db_hg, db_gla, stat_mix) = _mixer_tail_bwd(
        x0, p, o_list, dz2, y1, modx + tok, norms, onorms, wbh, wbg, wout, n_ctx_tiles, n_tiles, "mixer_tail_bwd")
    tok = reduce("mix", [_weight_grad(og_hg, db_hg, "grad_w_br_hg"), _weight_grad(og_gla, db_gla, "grad_w_br_gla"),
                         _weight_grad(merged, dy1, "grad_w_out")])
    lb_b = lb_full + tok
    gla_b = [(wgk, bias + tok) for wgk, bias in gla_side]
    (dhq_f, dhi_f, dhf_f, dlb_f, dhq_b, dhi_b, dhf_b, dlb_b) = _scan_bwd_both(
        p, lb_b, (st_hg_fw, st_hg_bw), d_ohg, n_ctx_chunks, "hg", "scan_hg_bwd")
    (dgq_f, dgk_f, dgv_f, dlr_f, dwgk_f, dbgk_f, dgq_b, dgk_b, dgv_b, dlr_b, dwgk_b, dbgk_b) = _scan_bwd_both(
        p, gla_b, (st_gla_fw, st_gla_bw), d_ogla, n_ctx_chunks, "gla", "scan_gla_bwd")
    pieces = [dhq_f, dhq_b, dhi_f, dhi_b, dhf_f, dhf_b, d_hgate, dgq_f, dgq_b, dgk_f, dgk_b, dgv_f, dgv_b, d_ggate,
              d_ghg, d_ggla, dlr_f, dlr_b]
    dp, grad_x, stat_in = _in_projection_bwd(ctx0, x0, dz2, modc, modx, norm_pre1, w_in_r, pieces, n_ctx_tiles,
                                             "in_projection_bwd")

    reduce("small", dict(stat_in=stat_in, stat_mix=stat_mix, stat_ffn=stat_ffn, dlb=(dlb_f, dlb_b),
                         dwgk=(dwgk_f, dwgk_b), dbgk=(dbgk_f, dbgk_b)))
    reduce("in", [_w_in_grad(dp, h1, w_in_r.shape[0], "grad_w_in")])
    return dict(
        loss_part=loss_part, grad_x=grad_x, stat_in=stat_in, stat_mix=stat_mix, stat_ffn=stat_ffn,
        dlb=(dlb_f, dlb_b), dwgk=(dwgk_f, dwgk_b), dbgk=(dbgk_f, dbgk_b))


def kernel(x, c, ctx, c_ctx, w_mod, b_mod, norm_pre1, norm_post1, norm_pre2, norm_post2, w_in, hg_lb, hg_onorm, gla_w_gk, gla_b_gk, gla_onorm, w_br_hg, w_br_gla, w_out, w_ff_gate, w_ff_up, w_ff_down, loss_target, m_c_ctx, m_w_mod, m_b_mod, m_norm_pre1, m_norm_post1, m_norm_pre2, m_norm_post2, m_w_in, m_hg_lb, m_hg_onorm, m_gla_w_gk, m_gla_b_gk, m_gla_onorm, m_w_br_hg, m_w_br_gla, m_w_out, m_w_ff_gate, m_w_ff_up, m_w_ff_down, v_c_ctx, v_w_mod, v_b_mod, v_norm_pre1, v_norm_post1, v_norm_pre2, v_norm_post2, v_w_in, v_hg_lb, v_hg_onorm, v_gla_w_gk, v_gla_b_gk, v_gla_onorm, v_w_br_hg, v_w_br_gla, v_w_out, v_w_ff_gate, v_w_ff_up, v_w_ff_down):
    seq, d = x.shape[1], x.shape[2]
    ctx_len = ctx.shape[1]
    assert seq % TM == 0 and ctx_len % TM == 0 and d == 2 * HW
    ax, ay, ac = lax.axis_index("x"), lax.axis_index("y"), lax.axis_index("c")
    chip = 2 * ax + ay
    dev = 2 * chip + ac
    c_arr = jnp.reshape(ac, (1,)).astype(jnp.int32)

    nc = d // 128
    pad8 = lambda a: jnp.pad(a, ((0, -a.shape[0] % 8), (0, 0)))
    small1 = jnp.concatenate([c.reshape(nc, 128), pad8(hg_lb.reshape(4, 128)), gla_w_gk.reshape(2 * RANK, 128),
                              pad8(gla_b_gk.reshape(2, 128))], axis=0)
    got1 = _allgather8(small1, "gather_small_params")
    c_all = got1[:, :nc, :].reshape(N_DEV, d)
    per_chip = got1[0::2]
    lb_full = per_chip[:, nc:nc + 4, :].transpose(1, 0, 2).reshape(4, HW)
    wgk_full = per_chip[:, nc + 8:nc + 8 + 2 * RANK, :].transpose(1, 0, 2).reshape(2, RANK, HW)
    bgk_full = per_chip[:, nc + 8 + 2 * RANK:nc + 10 + 2 * RANK, :].transpose(1, 0, 2).reshape(2, HW)
    wgk_pad = [jnp.zeros((128, HW), F32).at[dd * RANK:(dd + 1) * RANK].set(wgk_full[dd]) for dd in range(2)]
    bgk = [bgk_full[dd:dd + 1] for dd in range(2)]

    n_mod_cols = w_mod.shape[2]
    cond = jnp.concatenate([c_all, pad8(c_ctx.reshape(1, d))], axis=0)
    b_cols = lax.dynamic_slice(b_mod, (0, chip * n_mod_cols), (1, n_mod_cols))
    mod_part = _mod_forward(cond, w_mod[0], b_cols, "mod_forward")
    mod_got = _allgather8(mod_part, "gather_mod")
    mod_all = mod_got[0::2].transpose(1, 0, 2).reshape(16, N_CHIP * n_mod_cols)
    modx = pad8(lax.dynamic_slice(mod_all, (dev, 0), (1, N_MOD * d)).reshape(N_MOD, d))
    modc = pad8(mod_all[8].reshape(N_MOD, d))

    chip_arr = jnp.reshape(chip, (1,)).astype(jnp.int32)
    transposed = ("w_in", "w_ff_gate", "w_ff_up")
    view = lambda a, nm: a[0].T if nm in transposed else a[0]
    blocks = [_cast_into_blocks(chip_arr, view(w_, nm), "cast_" + nm) for w_, nm in (
        (w_in, "w_in"), (w_br_hg, "w_br_hg"), (w_br_gla, "w_br_gla"), (w_out, "w_out"), (w_ff_gate, "w_ff_gate"),
        (w_ff_up, "w_ff_up"), (w_ff_down, "w_ff_down"))]
    gathered_in = _gather_blocks(blocks[:1], "gather_w_in", after=[mod_got])
    sems, lands, token = _blocks_start(blocks[1:], "gather_rest_start", after=[gathered_in[0]])
    w_in_r = gathered_in[0].reshape(-1, d)

    norms = jnp.concatenate([norm_pre1, norm_post1, norm_pre2, norm_post2, jnp.zeros((4, d), F32)], axis=0)
    onorms = jnp.zeros((8, d), F32).at[0, :HD].set(hg_onorm[0]).at[1, :HD].set(gla_onorm[0])
    gla_side = [(wgk_pad[dd], bgk[dd]) for dd in range(2)]
    modx = modx + token[0, 0]
    front = _sample_front(x[0], ctx[0], modc, modx, norm_pre1, lb_full, gla_side, w_in_r)
    lands = _blocks_wait(sems, lands, front["o_list"], "gather_rest_wait")
    gathered = _blocks_finish(lands, "gather_rest_finish")
    wbh, wbg = _unblocked(gathered[0]), _unblocked(gathered[1])
    wout = gathered[2].reshape(d, d)
    wg, wu, wd = (gathered[i].reshape(-1, d) for i in (3, 4, 5))
    dff = wg.shape[0]
    groups = {"ffn": ["w_ff_gate", "w_ff_up", "w_ff_down"], "mix": ["w_br_hg", "w_br_gla", "w_out"], "in": ["w_in"]}
    row_sharded = {"w_out": d // N_CHIP, "w_ff_down": dff // N_CHIP, "w_ff_gate": dff // N_CHIP,
                   "w_ff_up": dff // N_CHIP, "w_in": w_in.shape[2]}
    in_flight = {}

    small = {}

    def reduce_small(stats):
        small2 = jnp.concatenate([
            stats["stat_in"], stats["stat_mix"], stats["stat_ffn"],
            jnp.concatenate(stats["dlb"], axis=1), jnp.concatenate(stats["dbgk"], axis=1),
            jnp.concatenate([stats["dwgk"][0][0:RANK], stats["dwgk"][1][RANK:2 * RANK]], axis=1)], axis=0)
        assert small2.shape[0] == SMALL_ROWS
        got2 = _allgather8(small2, "gather_small_grads")
        total, dmod_all, g_b_mod, g_lb_full = _reduce_small(got2, lb_full, "reduce_small")
        dmod_cols = lax.dynamic_slice(dmod_all, (0, chip * n_mod_cols), (16, n_mod_cols))
        g_w_mod, cctx_part = _mod_backward(cond, w_mod[0], dmod_cols, "mod_backward")
        got3 = _allgather8(cctx_part, "gather_c_ctx_grad")
        g_c_ctx = _c_ctx_grad(got3, c_ctx.reshape(1, d), "c_ctx_grad")
        small.update(total=total, g_b_mod=g_b_mod, g_lb_full=g_lb_full, g_w_mod=g_w_mod, g_c_ctx=g_c_ctx)

    def reduce(group, grads):
        if group == "small":
            return reduce_small(grads)
        nms = groups[group]
        full = [g.reshape(N_CHIP, row_sharded[nm], d) if nm in row_sharded else _blocked(g, N_CHIP)
                for g, nm in zip(grads, nms)]
        from_sibling = _send_other_half(full, "grads_to_sibling_" + group)
        pairs = [_pair_sum(c_arr, f, r_, "pair_sum_" + nm) for f, r_, nm in zip(full, from_sibling, nms)]
        after = [small["g_c_ctx"], small["total"]] if group == "in" else []
        sems_, pairs, lands_, token_ = _scatter_start(pairs, "grads_to_owner_start_" + group, after)
        in_flight[group] = (sems_, pairs, lands_, token_)
        return token_[0, 0]

    r = _sample_back(reduce, front, x[0], ctx[0], loss_target[0], modc, modx, norm_pre1, norms, onorms, lb_full,
                     gla_side, w_in_r, wbh, wbg, wout, wg, wu, wd)
    loss_part, grad_x, stat_in, stat_mix, stat_ffn = (r[k] for k in ("loss_part", "grad_x", "stat_in", "stat_mix",
                                                                     "stat_ffn"))
    (dlb_f, dlb_b), (dwgk_f, dwgk_b), (dbgk_f, dbgk_b) = r["dlb"], r["dwgk"], r["dbgk"]

    weights = dict(w_in=(w_in, m_w_in, v_w_in), w_br_hg=(w_br_hg, m_w_br_hg, v_w_br_hg),
                   w_br_gla=(w_br_gla, m_w_br_gla, v_w_br_gla), w_out=(w_out, m_w_out, v_w_out),
                   w_ff_gate=(w_ff_gate, m_w_ff_gate, v_w_ff_gate), w_ff_up=(w_ff_up, m_w_ff_up, v_w_ff_up),
                   w_ff_down=(w_ff_down, m_w_ff_down, v_w_ff_down))
    names = ["w_in", "w_br_hg", "w_br_gla", "w_out", "w_ff_gate", "w_ff_up", "w_ff_down"]
    big = {}

    def finish(group, after):
        sems_, pairs, lands_, _ = in_flight[group]
        pairs, lands_ = _scatter_wait(sems_, pairs, lands_, after, "grads_to_owner_wait_" + group)
        own_half = [_sum_owner(chip_arr, pr, g, "chip_sum_" + nm) for pr, g, nm in zip(pairs, lands_, groups[group])]
        other_half = _swap_with_sibling(own_half, "halves_to_sibling_" + group)
        for nm, own, oth in zip(groups[group], own_half, other_half):
            w_, m_, v_ = (view(a, nm) for a in weights[nm])
            res = _adamw_halves(c_arr, own, oth, w_, m_, v_, "adamw_" + nm)
            big[nm] = [r_.T[None] if nm in transposed else r_[None] for r_ in res]
        return big[groups[group][-1]][1]

    token_in = in_flight["in"][3]
    done_ffn = finish("ffn", [token_in])
    done_mix = finish("mix", [done_ffn])

    total, g_b_mod, g_lb_full, g_w_mod, g_c_ctx = (small[k] for k in ("total", "g_b_mod", "g_lb_full", "g_w_mod",
                                                                      "g_c_ctx"))
    g_pre1, g_post1, g_pre2, g_post2 = (total[r_:r_ + 1] for r_ in (ROW_PRE1, ROW_POST1, ROW_PRE2, ROW_POST2))
    g_hg_on, g_gla_on = total[ROW_ONORM:ROW_ONORM + 1, 0:HD], total[ROW_ONORM:ROW_ONORM + 1, HD:2 * HD]
    n_lb = hg_lb.shape[2]
    g_hg_lb = lax.dynamic_slice(g_lb_full, (0, chip * n_lb), (4, n_lb))
    g_bgk = lax.dynamic_slice(total[ROW_BGK:ROW_BGK + 1].reshape(2, HW), (0, chip * n_lb), (2, n_lb))
    g_wgk_full = total[ROW_WGK:ROW_WGK + RANK].reshape(RANK, 2, HW).transpose(1, 0, 2).reshape(2 * RANK, HW)
    g_wgk = lax.dynamic_slice(g_wgk_full, (0, chip * n_lb), (2 * RANK, n_lb))

    small_items = [
        (g_c_ctx, c_ctx.reshape(1, d), m_c_ctx.reshape(1, d), v_c_ctx.reshape(1, d)),
        (g_b_mod, b_mod, m_b_mod, v_b_mod),
        (g_pre1, norm_pre1, m_norm_pre1, v_norm_pre1),
        (g_post1, norm_post1, m_norm_post1, v_norm_post1),
        (g_pre2, norm_pre2, m_norm_pre2, v_norm_pre2),
        (g_post2, norm_post2, m_norm_post2, v_norm_post2),
        (g_hg_lb, hg_lb.reshape(4, n_lb), m_hg_lb.reshape(4, n_lb), v_hg_lb.reshape(4, n_lb)),
        (g_hg_on, hg_onorm, m_hg_onorm, v_hg_onorm),
        (g_wgk, gla_w_gk.reshape(2 * RANK, n_lb), m_gla_w_gk.reshape(2 * RANK, n_lb), v_gla_w_gk.reshape(2 * RANK, n_lb)),
        (g_bgk, gla_b_gk.reshape(2, n_lb), m_gla_b_gk.reshape(2, n_lb), v_gla_b_gk.reshape(2, n_lb)),
        (g_gla_on, gla_onorm, m_gla_onorm, v_gla_onorm),
    ]
    small_res = _adamw_whole(small_items, "adamw_small")
    mod_res = _adamw_tiled(g_w_mod, w_mod[0], m_w_mod[0], v_w_mod[0], "adamw_w_mod")
    finish("in", [done_mix, mod_res[0], small_res[0][0]])

    loss = total[ROW_LOSS, 0]

    shapes = dict(c_ctx=c_ctx.shape, b_mod=b_mod.shape, norm_pre1=norm_pre1.shape, norm_post1=norm_post1.shape,
                  norm_pre2=norm_pre2.shape, norm_post2=norm_post2.shape, hg_lb=hg_lb.shape, hg_onorm=hg_onorm.shape,
                  gla_w_gk=gla_w_gk.shape, gla_b_gk=gla_b_gk.shape, gla_onorm=gla_onorm.shape)
    small_names = ["c_ctx", "b_mod", "norm_pre1", "norm_post1", "norm_pre2", "norm_post2", "hg_lb", "hg_onorm",
                   "gla_w_gk", "gla_b_gk", "gla_onorm"]
    grads, deltas, new_m, new_v = {}, {}, {}, {}
    for nm, item, res in zip(small_names, small_items, small_res):
        grads[nm] = item[0].reshape(shapes[nm])
        deltas[nm], new_m[nm], new_v[nm] = (r.reshape(shapes[nm]) for r in res)
    grads["w_mod"] = g_w_mod[None]
    deltas["w_mod"], new_m["w_mod"], new_v["w_mod"] = (r[None] for r in mod_res)
    for nm in names:
        grads[nm], deltas[nm], new_m[nm], new_v[nm] = big[nm]
    order = ["c_ctx", "w_mod", "b_mod", "norm_pre1", "norm_post1", "norm_pre2", "norm_post2", "w_in", "hg_lb",
             "hg_onorm", "gla_w_gk", "gla_b_gk", "gla_onorm", "w_br_hg", "w_br_gla", "w_out", "w_ff_gate", "w_ff_up",
             "w_ff_down"]
    return (loss, grad_x[None], *[grads[n] for n in order], *[deltas[n] for n in order],
            *[new_m[n] for n in order], *[new_v[n] for n in order])


def _weight_grad_cols(xs, dy, n_cols, name, tn=512):
    rows = dy.shape[0]
    k = tk = xs.shape[1]

    return pl.pallas_call(
        functools.partial(_transposed_lhs_matmul), name=name, grid=(k // tk, n_cols // tn),
        out_shape=jax.ShapeDtypeStruct((k, n_cols), F32),
        in_specs=[pl.BlockSpec((rows, tk), lambda i, j: (0, i)), pl.BlockSpec((rows, tn), lambda i, j: (0, j))],
        out_specs=pl.BlockSpec((tk, tn), lambda i, j: (i, j)),
        scratch_shapes=[pltpu.VMEM((tk, rows), BF16)],
        compiler_params=_cparams(dimension_semantics=("parallel", "arbitrary")),
    )(xs, dy)
```

```python
import functools

import jax
import jax.numpy as jnp
from jax import lax
from jax.experimental import pallas as pl
from jax.experimental.pallas import tpu as pltpu

F32 = jnp.float32
BF16 = jnp.bfloat16
HIGHEST = lax.Precision.HIGHEST
MESH = pl.DeviceIdType.MESH

EPS = 1e-6
CHUNK = 64
SUB = 16
NSUB = CHUNK // SUB
NH = 4
HD = 128
HW = NH * HD
RANK = 16
GATE_NORM = 16.0
N_MOD = 6
TM = 256
TM_FFN = 128
N_DEV = 8
N_CHIP = 4
VMEM_LIMIT = 56 * 1024 * 1024

ADAM_LR = 0.001
ADAM_B1 = 0.9
ADAM_B2 = 0.999
ADAM_EPS = 1e-08
ADAM_WD = 0.01
ADAM_STEP = 10

VMEM_SPEC = pl.BlockSpec(memory_space=pltpu.VMEM)
ANY_SPEC = pl.BlockSpec(memory_space=pl.ANY)
HBM_SPEC = pl.BlockSpec(memory_space=pltpu.HBM)
SEM_SPEC = pl.BlockSpec(memory_space=pltpu.SEMAPHORE)
EFFECT = pltpu.SideEffectType.DATAFLOW_SIDE_EFFECTING


def _cparams(**kw):
    return pltpu.CompilerParams(vmem_limit_bytes=VMEM_LIMIT, **kw)


def _dot(a, b):
    return jnp.dot(a.astype(BF16), b.astype(BF16), preferred_element_type=F32)


def _dot_nt(a, b):
    return lax.dot_general(a.astype(BF16), b.astype(BF16), (((1,), (1,)), ((), ())), preferred_element_type=F32)


def _dot_tn(a, b):
    return lax.dot_general(a.astype(BF16), b.astype(BF16), (((0,), (0,)), ((), ())), preferred_element_type=F32)


def _sigmoid(x):
    return 1.0 / (1.0 + jnp.exp(-x))


def _silu(x):
    return x * _sigmoid(x)


def _dsilu(x):
    s = _sigmoid(x)
    return s * (1.0 + x * (1.0 - s))


def _log_sigmoid(x):
    return jnp.minimum(x, 0.0) - jnp.log(1.0 + jnp.exp(-jnp.abs(x)))


def _colsum(a):
    return jnp.sum(a, axis=0, keepdims=True)


def _rms(a):
    r = lax.rsqrt(jnp.mean(a * a, axis=-1, keepdims=True) + EPS)
    return a * r, r


def _rms_bwd(dn, n, r):
    return r * (dn - n * jnp.mean(dn * n, axis=-1, keepdims=True))


def _place():
    x, y, c = lax.axis_index("x"), lax.axis_index("y"), lax.axis_index("c")
    chips = [(1 - x, y), (x, 1 - y), (1 - x, 1 - y)]
    return x, y, c, chips


def _allgather8(v, name):
    rows, cols = v.shape

    def body(x_ref, out_ref, send_sems, recv_sems, local_sem):
        x, y, c, chips = _place()
        me, sibling = (x, y, c), (x, y, 1 - c)

        def blk(px, py, pc):
            return out_ref.at[4 * px + 2 * py + pc]

        def copy(k, block, to, src=None):
            return pltpu.make_async_remote_copy(
                src_ref=blk(*block) if src is None else src, dst_ref=blk(*block),
                send_sem=send_sems.at[k], recv_sem=recv_sems.at[k], device_id=to, device_id_type=MESH)

        mine = pltpu.make_async_copy(x_ref, blk(*me), local_sem)
        mine.start()
        first = [copy(0, me, sibling, src=x_ref)]
        first += [copy(1 + j, me, (*chip, c), src=x_ref) for j, chip in enumerate(chips)]
        for cp in first:
            cp.start()
        passed = [copy(4 + j, (*chip, c), sibling) for j, chip in enumerate(chips)]
        for j, chip in enumerate(chips):
            copy(1 + j, (*chip, c), me).wait_recv()
            passed[j].start()
        copy(0, sibling, me).wait_recv()
        for j, chip in enumerate(chips):
            copy(4 + j, (*chip, 1 - c), me).wait_recv()
        for cp in first + passed:
            cp.wait_send()
        mine.wait()

    return pl.pallas_call(
        body, name=name,
        out_shape=jax.ShapeDtypeStruct((N_DEV, rows, cols), v.dtype),
        in_specs=[VMEM_SPEC], out_specs=VMEM_SPEC,
        scratch_shapes=[pltpu.SemaphoreType.DMA((7,)), pltpu.SemaphoreType.DMA((7,)), pltpu.SemaphoreType.DMA],
    )(v)


def _cast_into_blocks(chip_arr, w, name):
    rows, cols = w.shape
    tr = _row_tile(rows, 16, 256)

    def body(chip_ref, w_ref, o_ref):
        o_ref[0] = w_ref[...].astype(BF16)

    return pl.pallas_call(
        body, name=name,
        grid_spec=pltpu.PrefetchScalarGridSpec(
            num_scalar_prefetch=1, grid=(rows // tr,),
            in_specs=[pl.BlockSpec((tr, cols), lambda i, chip_ref: (i, 0))],
            out_specs=pl.BlockSpec((1, tr, cols), lambda i, chip_ref: (chip_ref[0], i, 0))),
        out_shape=jax.ShapeDtypeStruct((N_CHIP, rows, cols), BF16),
        compiler_params=_cparams(dimension_semantics=("parallel",)),
    )(chip_arr, w)


def _halved_by_rows(shape):
    return (shape[1] // 2) % 16 == 0


def _half_of(ref, pc, block=None):
    lead = slice(None) if block is None else block
    if _halved_by_rows(ref.shape):
        h = ref.shape[1] // 2
        return ref.at[lead, pl.ds(pl.multiple_of(pc * h, 16), h), :]
    h = ref.shape[2] // 2
    return ref.at[lead, :, pl.ds(pl.multiple_of(pc * h, 128), h)]


def _half_shape(shape):
    return (shape[0], shape[1] // 2, shape[2]) if _halved_by_rows(shape) else (shape[0], shape[1], shape[2] // 2)


def _half_rows(ref, chip_id, pc):
    return _half_of(ref, pc, chip_id)


def _gather_blocks(lands, name, after=()):
    n = len(lands)
    n_in = n + len(after)

    def body(*refs):
        outs = refs[n_in:n_in + n]
        send_sems, recv_sems = refs[n_in + n:]
        x, y, c, chips = _place()
        me_chip = 2 * x + y
        sibling = (x, y, 1 - c)

        def copy(k, j, chip_id, pc, to):
            return pltpu.make_async_remote_copy(
                src_ref=_half_rows(outs[k], chip_id, pc), dst_ref=_half_rows(outs[k], chip_id, pc),
                send_sem=send_sems.at[k, j], recv_sem=recv_sems.at[k, j], device_id=to, device_id_type=MESH)

        started = []
        for k in range(n):
            for j, chip in enumerate(chips):
                cp = copy(k, j, me_chip, c, (*chip, c))
                cp.start()
                started.append(cp)
        for k in range(n):
            for j, (px, py) in enumerate(chips):
                copy(k, j, 2 * px + py, c, sibling).wait_recv()
                cp = copy(k, 3 + j, 2 * px + py, c, sibling)
                cp.start()
                started.append(cp)
        for k in range(n):
            for j, (px, py) in enumerate(chips):
                copy(k, 3 + j, 2 * px + py, 1 - c, sibling).wait_recv()
        for cp in started:
            cp.wait_send()

    return pl.pallas_call(
        body, name=name,
        out_shape=[jax.ShapeDtypeStruct(l.shape, l.dtype) for l in lands],
        in_specs=[ANY_SPEC] * n_in, out_specs=[ANY_SPEC] * n,
        input_output_aliases={i: i for i in range(n)},
        scratch_shapes=[pltpu.SemaphoreType.DMA((n, 6)), pltpu.SemaphoreType.DMA((n, 6))],
    )(*lands, *after)


def _hbm(a):
    return pltpu.with_memory_space_constraint(a, pltpu.HBM)


def _blocks_start(lands, name, after=()):
    n = len(lands)
    n_sem = 3 * n
    first = n + len(after)

    def body(*refs):
        lnd = refs[:n]
        send_sems, recv_sems = refs[first:first + n_sem], refs[first + n_sem:first + 2 * n_sem]
        token = refs[-1]
        x, y, c, chips = _place()
        me_chip = 2 * x + y
        for k in range(n):
            for j, chip in enumerate(chips):
                pltpu.make_async_remote_copy(
                    src_ref=_half_rows(lnd[k], me_chip, c), dst_ref=_half_rows(lnd[k], me_chip, c),
                    send_sem=send_sems[3 * k + j], recv_sem=recv_sems[3 * k + j],
                    device_id=(*chip, c), device_id_type=MESH).start()
        token[...] = jnp.zeros_like(token)

    out = pl.pallas_call(
        body, name=name,
        out_shape=(*[pltpu.SemaphoreType.DMA(())] * (2 * n_sem),
                   *[pltpu.HBM(l.shape, l.dtype) for l in lands],
                   jax.ShapeDtypeStruct((8, 128), F32)),
        in_specs=[HBM_SPEC] * n + [ANY_SPEC] * len(after),
        out_specs=(*[SEM_SPEC] * (2 * n_sem), *[HBM_SPEC] * n, VMEM_SPEC),
        input_output_aliases={i: 2 * n_sem + i for i in range(n)},
        compiler_params=pltpu.CompilerParams(has_side_effects=EFFECT),
    )(*[_hbm(l) for l in lands], *after)
    return list(out[:2 * n_sem]), list(out[2 * n_sem:2 * n_sem + n]), out[-1]


def _blocks_wait(sems, lands, after, name):
    n = len(lands)
    n_sem = 3 * n

    def body(*refs):
        lnd = refs[:n]
        s_sems, r_sems = refs[n:n + n_sem], refs[n + n_sem:n + 2 * n_sem]
        x, y, c, chips = _place()
        me_chip = 2 * x + y
        for k in range(n):
            for j, (px, py) in enumerate(chips):
                cp = pltpu.make_async_remote_copy(
                    src_ref=_half_rows(lnd[k], me_chip, c), dst_ref=_half_rows(lnd[k], 2 * px + py, c),
                    send_sem=s_sems[3 * k + j], recv_sem=r_sems[3 * k + j],
                    device_id=(px, py, c), device_id_type=MESH)
                cp.wait_send()
                cp.wait_recv()

    out = pl.pallas_call(
        body, name=name,
        out_shape=tuple(pltpu.HBM(l.shape, l.dtype) for l in lands),
        in_specs=[HBM_SPEC] * n + [SEM_SPEC] * (2 * n_sem) + [ANY_SPEC] * len(after),
        out_specs=[HBM_SPEC] * n,
        input_output_aliases={i: i for i in range(n)},
        compiler_params=pltpu.CompilerParams(has_side_effects=EFFECT),
    )(*lands, *sems, *after)
    return list(out)


def _blocks_finish(lands, name):
    n = len(lands)

    def body(*refs):
        lnd = refs[n:2 * n]
        send_sems, recv_sems = refs[2 * n:]
        x, y, c, chips = _place()
        sibling = (x, y, 1 - c)

        def copy(k, j, chip_id, pc):
            return pltpu.make_async_remote_copy(
                src_ref=_half_rows(lnd[k], chip_id, pc), dst_ref=_half_rows(lnd[k], chip_id, pc),
                send_sem=send_sems.at[k, j], recv_sem=recv_sems.at[k, j], device_id=sibling, device_id_type=MESH)

        started = []
        for k in range(n):
            for j, (px, py) in enumerate(chips):
                cp = copy(k, j, 2 * px + py, c)
                cp.start()
                started.append(cp)
        for k in range(n):
            for j, (px, py) in enumerate(chips):
                copy(k, j, 2 * px + py, 1 - c).wait_recv()
        for cp in started:
            cp.wait_send()

    out = pl.pallas_call(
        body, name=name,
        out_shape=[jax.ShapeDtypeStruct(l.shape, l.dtype) for l in lands],
        in_specs=[ANY_SPEC] * n, out_specs=[ANY_SPEC] * n,
        input_output_aliases={i: i for i in range(n)},
        scratch_shapes=[pltpu.SemaphoreType.DMA((n, 3)), pltpu.SemaphoreType.DMA((n, 3))],
    )(*lands)
    return list(out)


def _gather_start(shards, name):
    n = len(shards)
    n_sem = 3 * n

    def body(*refs):
        ins, lands = refs[:n], refs[n:2 * n]
        send_sems, recv_sems = refs[2 * n:2 * n + n_sem], refs[2 * n + n_sem:2 * n + 2 * n_sem]
        token = refs[-1]
        x, y, c, chips = _place()
        me_chip = 2 * x + y
        for k in range(n):
            h = shards[k].shape[0] // 2
            rows = pl.ds(pl.multiple_of(c * h, 8), h)
            for j, chip in enumerate(chips):
                pltpu.make_async_remote_copy(
                    src_ref=ins[k].at[rows, :], dst_ref=lands[k].at[me_chip, rows, :],
                    send_sem=send_sems[3 * k + j], recv_sem=recv_sems[3 * k + j],
                    device_id=(*chip, c), device_id_type=MESH).start()
        token[...] = jnp.zeros_like(token)

    lands = [_hbm(lax.empty((N_CHIP,) + s.shape, s.dtype)) for s in shards]
    out = pl.pallas_call(
        body, name=name,
        out_shape=(*[pltpu.SemaphoreType.DMA(())] * (2 * n_sem),
                   *[pltpu.HBM(s.shape, s.dtype) for s in shards],
                   *[pltpu.HBM(l.shape, l.dtype) for l in lands],
                   jax.ShapeDtypeStruct((8, 128), F32)),
        in_specs=[HBM_SPEC] * (2 * n),
        out_specs=(*[SEM_SPEC] * (2 * n_sem), *[HBM_SPEC] * (2 * n), VMEM_SPEC),
        input_output_aliases={i: 2 * n_sem + i for i in range(2 * n)},
        compiler_params=pltpu.CompilerParams(has_side_effects=EFFECT),
    )(*[_hbm(s) for s in shards], *lands)
    sems = list(out[:2 * n_sem])
    return sems, list(out[2 * n_sem:2 * n_sem + n]), list(out[2 * n_sem + n:2 * n_sem + 2 * n]), out[-1]


def _gather_wait(sems, shards, lands, after, name):
    n = len(shards)
    n_sem = 3 * n

    def body(*refs):
        ins, lnd = refs[:n], refs[n:2 * n]
        s_sems, r_sems = refs[2 * n:2 * n + n_sem], refs[2 * n + n_sem:2 * n + 2 * n_sem]
        x, y, c, chips = _place()
        for k in range(n):
            h = shards[k].shape[0] // 2
            rows = pl.ds(pl.multiple_of(c * h, 8), h)
            for j, (px, py) in enumerate(chips):
                cp = pltpu.make_async_remote_copy(
                    src_ref=ins[k].at[rows, :], dst_ref=lnd[k].at[2 * px + py, rows, :],
                    send_sem=s_sems[3 * k + j], recv_sem=r_sems[3 * k + j],
                    device_id=(px, py, c), device_id_type=MESH)
                cp.wait_send()
                cp.wait_recv()

    out = pl.pallas_call(
        body, name=name,
        out_shape=(*[pltpu.HBM(s.shape, s.dtype) for s in shards], *[pltpu.HBM(l.shape, l.dtype) for l in lands]),
        in_specs=[HBM_SPEC] * (2 * n) + [SEM_SPEC] * (2 * n_sem) + [ANY_SPEC],
        out_specs=[HBM_SPEC] * (2 * n),
        input_output_aliases={i: i for i in range(2 * n)},
        compiler_params=pltpu.CompilerParams(has_side_effects=EFFECT),
    )(*shards, *lands, *sems, after)
    return list(out[:n]), list(out[n:])


def _gather_finish(shards, lands, name):
    n = len(shards)

    def body(*refs):
        ins, lnd = refs[:n], refs[2 * n:3 * n]
        send_sems, recv_sems, local_sems = refs[3 * n:]
        x, y, c, chips = _place()
        me_chip = 2 * x + y
        sibling = (x, y, 1 - c)

        def half(k, chip_id, pc):
            h = shards[k].shape[0] // 2
            return lnd[k].at[chip_id, pl.ds(pl.multiple_of(pc * h, 8), h), :]

        def copy(k, j, chip_id, pc):
            return pltpu.make_async_remote_copy(
                src_ref=half(k, chip_id, pc), dst_ref=half(k, chip_id, pc),
                send_sem=send_sems.at[k, j], recv_sem=recv_sems.at[k, j], device_id=sibling, device_id_type=MESH)

        locals_, started = [], []
        for k in range(n):
            cp = pltpu.make_async_copy(ins[k], lnd[k].at[me_chip], local_sems.at[k])
            cp.start()
            locals_.append(cp)
            for j, (px, py) in enumerate(chips):
                cp = copy(k, j, 2 * px + py, c)
                cp.start()
                started.append(cp)
        for k in range(n):
            for j, (px, py) in enumerate(chips):
                copy(k, j, 2 * px + py, 1 - c).wait_recv()
        for cp in started:
            cp.wait_send()
        for cp in locals_:
            cp.wait()

    out = pl.pallas_call(
        body, name=name,
        out_shape=[jax.ShapeDtypeStruct(l.shape, l.dtype) for l in lands],
        in_specs=[ANY_SPEC] * (2 * n), out_specs=[ANY_SPEC] * n,
        input_output_aliases={n + i: i for i in range(n)},
        scratch_shapes=[pltpu.SemaphoreType.DMA((n, 3)), pltpu.SemaphoreType.DMA((n, 3)),
                        pltpu.SemaphoreType.DMA((n,))],
    )(*shards, *lands)
    return list(out)


def _send_other_half(arrs, name):
    n = len(arrs)

    def body(*refs):
        ins, outs = refs[:n], refs[n:2 * n]
        send_sems, recv_sems = refs[2 * n:]
        x, y, c, _ = _place()
        cps = []
        for k in range(n):
            cp = pltpu.make_async_remote_copy(
                src_ref=_half_of(ins[k], 1 - c), dst_ref=outs[k],
                send_sem=send_sems.at[k], recv_sem=recv_sems.at[k], device_id=(x, y, 1 - c), device_id_type=MESH)
            cp.start()
            cps.append(cp)
        for cp in cps:
            cp.wait()

    return pl.pallas_call(
        body, name=name,
        out_shape=[jax.ShapeDtypeStruct(_half_shape(a.shape), a.dtype) for a in arrs],
        in_specs=[ANY_SPEC] * n, out_specs=[ANY_SPEC] * n,
        scratch_shapes=[pltpu.SemaphoreType.DMA((n,)), pltpu.SemaphoreType.DMA((n,))],
    )(*arrs)


def _blocks_to_owner(arrs, name):
    n = len(arrs)

    def body(*refs):
        ins, outs = refs[:n], refs[n:2 * n]
        send_sems, recv_sems, local_sems = refs[2 * n:]
        x, y, c, chips = _place()
        me_chip = 2 * x + y
        locals_, started = [], []
        for k in range(n):
            cp = pltpu.make_async_copy(ins[k].at[me_chip], outs[k].at[me_chip], local_sems.at[k])
            cp.start()
            locals_.append(cp)

        def copy(k, j, src_block, dst_slot, to):
            return pltpu.make_async_remote_copy(
                src_ref=ins[k].at[src_block], dst_ref=outs[k].at[dst_slot],
                send_sem=send_sems.at[k, j], recv_sem=recv_sems.at[k, j], device_id=to, device_id_type=MESH)

        for k in range(n):
            for j, (px, py) in enumerate(chips):
                cp = copy(k, j, 2 * px + py, me_chip, (px, py, c))
                cp.start()
                started.append(cp)
        for k in range(n):
            for j, (px, py) in enumerate(chips):
                copy(k, j, me_chip, 2 * px + py, (px, py, c)).wait_recv()
        for cp in started:
            cp.wait_send()
        for cp in locals_:
            cp.wait()

    return pl.pallas_call(
        body, name=name,
        out_shape=[jax.ShapeDtypeStruct(a.shape, a.dtype) for a in arrs],
        in_specs=[ANY_SPEC] * n, out_specs=[ANY_SPEC] * n,
        scratch_shapes=[pltpu.SemaphoreType.DMA((n, 3)), pltpu.SemaphoreType.DMA((n, 3)),
                        pltpu.SemaphoreType.DMA((n,))],
    )(*arrs)


def _scatter_blocks(arrs, name):
    n = len(arrs)

    def body(*refs):
        ins, outs = refs[:n], refs[n:2 * n]
        send_sems, recv_sems = refs[2 * n:]
        x, y, c, chips = _place()
        me_chip = 2 * x + y

        def copy(k, j, src_block, dst_slot, to):
            return pltpu.make_async_remote_copy(
                src_ref=ins[k].at[src_block], dst_ref=outs[k].at[dst_slot],
                send_sem=send_sems.at[k, j], recv_sem=recv_sems.at[k, j], device_id=to, device_id_type=MESH)

        started = []
        for k in range(n):
            for j, (px, py) in enumerate(chips):
                cp = copy(k, j, 2 * px + py, me_chip, (px, py, c))
                cp.start()
                started.append(cp)
        for k in range(n):
            for j, (px, py) in enumerate(chips):
                copy(k, j, me_chip, 2 * px + py, (px, py, c)).wait_recv()
        for cp in started:
            cp.wait_send()

    return pl.pallas_call(
        body, name=name,
        out_shape=[jax.ShapeDtypeStruct(a.shape, a.dtype) for a in arrs],
        in_specs=[ANY_SPEC] * n, out_specs=[ANY_SPEC] * n,
        scratch_shapes=[pltpu.SemaphoreType.DMA((n, 3)), pltpu.SemaphoreType.DMA((n, 3))],
    )(*arrs)


def _scatter_start(arrs, name, after=()):
    n = len(arrs)
    n_sem = 3 * n
    first = 2 * n + len(after)

    def body(*refs):
        ins, lnd = refs[:n], refs[n:2 * n]
        send_sems, recv_sems = refs[first:first + n_sem], refs[first + n_sem:first + 2 * n_sem]
        token = refs[-1]
        x, y, c, chips = _place()
        me_chip = 2 * x + y
        for k in range(n):
            for j, (px, py) in enumerate(chips):
                pltpu.make_async_remote_copy(
                    src_ref=ins[k].at[2 * px + py], dst_ref=lnd[k].at[me_chip],
                    send_sem=send_sems[3 * k + j], recv_sem=recv_sems[3 * k + j],
                    device_id=(px, py, c), device_id_type=MESH).start()
        token[...] = jnp.zeros_like(token)

    lands = [_hbm(lax.empty(a.shape, a.dtype)) for a in arrs]
    out = pl.pallas_call(
        body, name=name,
        out_shape=(*[pltpu.SemaphoreType.DMA(())] * (2 * n_sem),
                   *[pltpu.HBM(a.shape, a.dtype) for a in arrs], *[pltpu.HBM(a.shape, a.dtype) for a in arrs],
                   jax.ShapeDtypeStruct((8, 128), F32)),
        in_specs=[HBM_SPEC] * (2 * n) + [ANY_SPEC] * len(after),
        out_specs=(*[SEM_SPEC] * (2 * n_sem), *[HBM_SPEC] * (2 * n), VMEM_SPEC),
        input_output_aliases={i: 2 * n_sem + i for i in range(2 * n)},
        compiler_params=pltpu.CompilerParams(has_side_effects=EFFECT),
    )(*[_hbm(a) for a in arrs], *lands, *after)
    base = 2 * n_sem
    return list(out[:base]), list(out[base:base + n]), list(out[base + n:base + 2 * n]), out[-1]


def _scatter_wait(sems, arrs, lands, after, name):
    n = len(arrs)
    n_sem = 3 * n

    def body(*refs):
        ins, lnd = refs[:n], refs[n:2 * n]
        s_sems, r_sems = refs[2 * n:2 * n + n_sem], refs[2 * n + n_sem:2 * n + 2 * n_sem]
        x, y, c, chips = _place()
        for k in range(n):
            for j, (px, py) in enumerate(chips):
                cp = pltpu.make_async_remote_copy(
                    src_ref=ins[k].at[2 * px + py], dst_ref=lnd[k].at[2 * px + py],
                    send_sem=s_sems[3 * k + j], recv_sem=r_sems[3 * k + j],
                    device_id=(px, py, c), device_id_type=MESH)
                cp.wait_send()
                cp.wait_recv()

    out = pl.pallas_call(
        body, name=name,
        out_shape=tuple(pltpu.HBM(a.shape, a.dtype) for a in list(arrs) + list(lands)),
        in_specs=[HBM_SPEC] * (2 * n) + [SEM_SPEC] * (2 * n_sem) + [ANY_SPEC] * len(after),
        out_specs=[HBM_SPEC] * (2 * n),
        input_output_aliases={i: i for i in range(2 * n)},
        compiler_params=pltpu.CompilerParams(has_side_effects=EFFECT),
    )(*arrs, *lands, *sems, *after)
    return list(out[:n]), list(out[n:])


def _sum_owner(chip_arr, pairs, got, name):
    nb, h, cols = got.shape
    tr = _row_tile(h, 16, 256)

    def body(chip_ref, own_ref, a_ref, b_ref, c_ref, o_ref):
        o_ref[...] = ((own_ref[0].astype(F32) + a_ref[0].astype(F32)) + b_ref[0].astype(F32)) + c_ref[0].astype(F32)

    def slot(off):
        return pl.BlockSpec((1, tr, cols), lambda i, chip_ref: ((chip_ref[0] + off) % N_CHIP, i, 0))

    return pl.pallas_call(
        body, name=name,
        grid_spec=pltpu.PrefetchScalarGridSpec(
            num_scalar_prefetch=1, grid=(h // tr,),
            in_specs=[slot(0), slot(1), slot(2), slot(3)],
            out_specs=pl.BlockSpec((tr, cols), lambda i, chip_ref: (i, 0))),
        out_shape=jax.ShapeDtypeStruct((h, cols), F32),
        compiler_params=_cparams(dimension_semantics=("parallel",)),
    )(chip_arr, pairs, got, got, got)


def _swap_with_sibling(arrs, name):
    n = len(arrs)

    def body(*refs):
        ins, outs = refs[:n], refs[n:2 * n]
        send_sems, recv_sems = refs[2 * n:]
        x, y, c, _ = _place()
        cps = []
        for k in range(n):
            cp = pltpu.make_async_remote_copy(
                src_ref=ins[k], dst_ref=outs[k], send_sem=send_sems.at[k], recv_sem=recv_sems.at[k],
                device_id=(x, y, 1 - c), device_id_type=MESH)
            cp.start()
            cps.append(cp)
        for cp in cps:
            cp.wait()

    return pl.pallas_call(
        body, name=name,
        out_shape=[jax.ShapeDtypeStruct(a.shape, a.dtype) for a in arrs],
        in_specs=[ANY_SPEC] * n, out_specs=[ANY_SPEC] * n,
        scratch_shapes=[pltpu.SemaphoreType.DMA((n,)), pltpu.SemaphoreType.DMA((n,))],
    )(*arrs)


def _row_tile(h, mult=8, cap=128):
    for t in range(cap - cap % mult, mult - 1, -mult):
        if h % t == 0:
            return t
    if mult > 8:
        return _row_tile(h, 8, cap)
    raise ValueError(h)


def _cast_bf16(a, name):
    rows, cols = a.shape
    tr = _row_tile(rows, 16, 256)

    def body(a_ref, o_ref):
        o_ref[...] = a_ref[...].astype(BF16)

    return pl.pallas_call(
        body, name=name, grid=(rows // tr,),
        out_shape=jax.ShapeDtypeStruct(a.shape, BF16),
        in_specs=[pl.BlockSpec((tr, cols), lambda i: (i, 0))],
        out_specs=pl.BlockSpec((tr, cols), lambda i: (i, 0)),
        compiler_params=_cparams(dimension_semantics=("parallel",)),
    )(a)


def _pair_sum(c_arr, full, recv, name):
    nb, rows, cols = full.shape

    def body(c_ref, f_ref, r_ref, o_ref):
        o_ref[...] = (f_ref[...] + r_ref[...]).astype(BF16)

    if _halved_by_rows(full.shape):
        h = rows // 2
        tr = _row_tile(h, 16, 256)
        steps = h // tr
        own = pl.BlockSpec((1, tr, cols), lambda b, i, c_ref: (b, c_ref[0] * steps + i, 0))
        half = pl.BlockSpec((1, tr, cols), lambda b, i, c_ref: (b, i, 0))
    else:
        steps = 1
        own = pl.BlockSpec((1, rows, cols // 2), lambda b, i, c_ref: (b, 0, c_ref[0]))
        half = pl.BlockSpec((1, rows, cols // 2), lambda b, i, c_ref: (b, 0, 0))
    return pl.pallas_call(
        body, name=name,
        grid_spec=pltpu.PrefetchScalarGridSpec(
            num_scalar_prefetch=1, grid=(nb, steps), in_specs=[own, half], out_specs=half),
        out_shape=jax.ShapeDtypeStruct(_half_shape(full.shape), BF16),
        compiler_params=_cparams(dimension_semantics=("parallel", "parallel")),
    )(c_arr, full, recv)


def _sum_chips(got, name):
    nb, h, cols = got.shape
    tr = _row_tile(h, 16, 256)

    def body(g_ref, o_ref):
        g = g_ref[...].astype(F32)
        o_ref[...] = ((g[0] + g[1]) + g[2]) + g[3]

    return pl.pallas_call(
        body, name=name, grid=(h // tr,),
        out_shape=jax.ShapeDtypeStruct((h, cols), F32),
        in_specs=[pl.BlockSpec((nb, tr, cols), lambda i: (0, i, 0))],
        out_specs=pl.BlockSpec((tr, cols), lambda i: (i, 0)),
        compiler_params=_cparams(dimension_semantics=("parallel",)),
    )(got)


def _adam_math(g, w, m, v):
    m1 = ADAM_B1 * m + (1.0 - ADAM_B1) * g
    v1 = ADAM_B2 * v + (1.0 - ADAM_B2) * (g * g)
    m_hat = m1 / (1.0 - ADAM_B1 ** ADAM_STEP)
    v_hat = v1 / (1.0 - ADAM_B2 ** ADAM_STEP)
    delta = -ADAM_LR * (m_hat / (jnp.sqrt(v_hat) + ADAM_EPS) + ADAM_WD * w)
    return delta, m1, v1


def _adamw_halves(c_arr, own, other, w, m, v, name):
    rows, cols = w.shape
    by_rows = own.shape[1] == cols

    def body(c_ref, own_ref, oth_ref, w_ref, m_ref, v_ref, g_out, d_out, m_out, v_out):
        if by_rows:
            g = jnp.where(pl.program_id(0) == c_ref[0], own_ref[...], oth_ref[...])
        else:
            own_, oth_ = own_ref[...], oth_ref[...]
            g = jnp.where(c_ref[0] == 0, jnp.concatenate([own_, oth_], axis=1), jnp.concatenate([oth_, own_], axis=1))
        d, m1, v1 = _adam_math(g, w_ref[...], m_ref[...], v_ref[...])
        g_out[...] = g
        d_out[...] = d
        m_out[...] = m1
        v_out[...] = v1

    if by_rows:
        h = rows // 2
        tr = _row_tile(h)
        steps = h // tr
        grid = (2, steps)
        half_spec = pl.BlockSpec((tr, cols), lambda p, i, c_ref: (i, 0))
        full_spec = pl.BlockSpec((tr, cols), lambda p, i, c_ref: (p * steps + i, 0))
    else:
        tr = _row_tile(rows)
        grid = (1, rows // tr)
        half_spec = pl.BlockSpec((tr, cols // 2), lambda p, i, c_ref: (i, 0))
        full_spec = pl.BlockSpec((tr, cols), lambda p, i, c_ref: (i, 0))
    return pl.pallas_call(
        body, name=name,
        grid_spec=pltpu.PrefetchScalarGridSpec(
            num_scalar_prefetch=1, grid=grid,
            in_specs=[half_spec, half_spec, full_spec, full_spec, full_spec],
            out_specs=[full_spec] * 4),
        out_shape=[jax.ShapeDtypeStruct(w.shape, F32)] * 4,
        compiler_params=_cparams(dimension_semantics=("parallel", "parallel")),
    )(c_arr, own, other, w, m, v)


def _adamw_whole(items, name):
    n = len(items)

    def body(*refs):
        ins, outs = refs[:4 * n], refs[4 * n:]
        for k in range(n):
            g, w, m, v = (r[...] for r in ins[4 * k:4 * k + 4])
            d, m1, v1 = _adam_math(g, w, m, v)
            outs[3 * k][...] = d
            outs[3 * k + 1][...] = m1
            outs[3 * k + 2][...] = v1

    flat = [a for it in items for a in it]
    shapes = [jax.ShapeDtypeStruct(it[1].shape, F32) for it in items for _ in range(3)]
    out = pl.pallas_call(
        body, name=name, out_shape=shapes,
        in_specs=[VMEM_SPEC] * (4 * n), out_specs=[VMEM_SPEC] * (3 * n),
        compiler_params=_cparams(),
    )(*flat)
    return [tuple(out[3 * k:3 * k + 3]) for k in range(n)]


def _adamw_tiled(g, w, m, v, name):
    rows, cols = w.shape
    tr = _row_tile(rows)

    def body(g_ref, w_ref, m_ref, v_ref, d_out, m_out, v_out):
        d, m1, v1 = _adam_math(g_ref[...], w_ref[...], m_ref[...], v_ref[...])
        d_out[...] = d
        m_out[...] = m1
        v_out[...] = v1

    spec = pl.BlockSpec((tr, cols), lambda i: (i, 0))
    return pl.pallas_call(
        body, name=name, grid=(rows // tr,),
        out_shape=[jax.ShapeDtypeStruct(w.shape, F32)] * 3,
        in_specs=[spec] * 4, out_specs=[spec] * 3,
        compiler_params=_cparams(dimension_semantics=("parallel",)),
    )(g, w, m, v)


def _mod_forward(cond, w_mod, b_mod_cols, name):
    def body(c_ref, w_ref, b_ref, o_ref):
        o_ref[...] = _dot(_silu(c_ref[...]), w_ref[...]) + b_ref[...]

    return pl.pallas_call(
        body, name=name, out_shape=jax.ShapeDtypeStruct((cond.shape[0], w_mod.shape[1]), F32),
        in_specs=[VMEM_SPEC] * 3, out_specs=VMEM_SPEC, compiler_params=_cparams(),
    )(cond, w_mod, b_mod_cols)


def _mod_backward(cond, w_mod, dmod_cols, name):
    def body(c_ref, w_ref, d_ref, gw_ref, gc_ref):
        s = _silu(c_ref[...])
        d = d_ref[...]
        gw_ref[...] = _dot_tn(s, d)
        gc_ref[...] = _dot_nt(d[8:16, :], w_ref[...])

    return pl.pallas_call(
        body, name=name,
        out_shape=[jax.ShapeDtypeStruct(w_mod.shape, F32), jax.ShapeDtypeStruct((8, w_mod.shape[0]), F32)],
        in_specs=[VMEM_SPEC] * 3, out_specs=[VMEM_SPEC] * 2, compiler_params=_cparams(),
    )(cond, w_mod, dmod_cols)


def _col_chunks(width, step=512):
    return [(s, min(step, width - s)) for s in range(0, width, step)]


def _w_in_row(p_off):
    if p_off < 9 * HW:
        return p_off
    return 9 * HW if p_off == OFF_LR else p_off + 2 * RANK


def _in_projection(ctx0, x0, modc, modx, pre1, w_t, n_ctx_tiles, name):
    d = x0.shape[1]
    rows = ctx0.shape[0] + x0.shape[0]
    width = P_WIDTH

    def body(ctx_ref, x_ref, modc_ref, modx_ref, pre_ref, w_ref, h_ref, p_ref):
        is_ctx = pl.program_id(0) < n_ctx_tiles
        n, _ = _rms(jnp.where(is_ctx, ctx_ref[...], x_ref[...]))
        shift = jnp.where(is_ctx, modc_ref[0:1, :], modx_ref[0:1, :])
        scale = jnp.where(is_ctx, modc_ref[1:2, :], modx_ref[1:2, :])
        h = (n * pre_ref[...] * (1.0 + scale) + shift).astype(BF16)
        h_ref[...] = h
        for s, w in _col_chunks(width):
            p_ref[:, s:s + w] = _dot_nt(h, w_ref[_w_in_row(s):_w_in_row(s) + w, :])

    row = lambda i: (i, 0)
    fixed = lambda i: (0, 0)
    return pl.pallas_call(
        body, name=name, grid=(rows // TM,),
        out_shape=[jax.ShapeDtypeStruct((rows, d), BF16), jax.ShapeDtypeStruct((rows, width), F32)],
        in_specs=[pl.BlockSpec((TM, d), lambda i: (jnp.minimum(i, n_ctx_tiles - 1), 0)),
                  pl.BlockSpec((TM, d), lambda i: (jnp.maximum(i - n_ctx_tiles, 0), 0)),
                  pl.BlockSpec((8, d), fixed), pl.BlockSpec((8, d), fixed), pl.BlockSpec((1, d), fixed), VMEM_SPEC],
        out_specs=[pl.BlockSpec((TM, d), row), pl.BlockSpec((TM, width), row)],
        compiler_params=_cparams(dimension_semantics=("parallel",)),
    )(ctx0, x0, modc, modx, pre1, w_t)


C_HQ, C_HI, C_HF_FW, C_HF_BW, C_HGATE, C_GQ, C_GK, C_GV, C_GGATE = range(9)
OFF_GATE_HG = 9 * HW
OFF_LR = 13 * HW
P_WIDTH = OFF_LR + 128


def _head_norm_fwd(o, w):
    outs, ns, rs = [], [], []
    for h in range(NH):
        n, r = _rms(o[:, h * HD:(h + 1) * HD])
        ns.append(n)
        rs.append(r)
        outs.append(n * w)
    return jnp.concatenate(outs, axis=1), ns, rs


def _mixer_tail(z, o_hg, o_gla, p_hgate, p_ggate, p_gate_hg, p_gate_gla, hg_on, gla_on, wbh, wbg, wout):
    on_hg, n_hg, r_hg = _head_norm_fwd(o_hg, hg_on)
    on_gla, n_gla, r_gla = _head_norm_fwd(o_gla, gla_on)
    og_hg = (on_hg * _silu(p_hgate)).astype(BF16)
    og_gla = (on_gla * _silu(p_ggate)).astype(BF16)
    b_hg = jnp.dot(og_hg, wbh, preferred_element_type=F32)
    b_gla = jnp.dot(og_gla, wbg, preferred_element_type=F32)
    s_hg = _sigmoid(p_gate_hg)
    s_gla = _sigmoid(p_gate_gla)
    merged = (s_hg * b_hg + s_gla * b_gla).astype(BF16)
    y1 = jnp.dot(merged, wout, preferred_element_type=F32)
    return dict(on_hg=on_hg, n_hg=n_hg, r_hg=r_hg, on_gla=on_gla, n_gla=n_gla, r_gla=r_gla, og_hg=og_hg,
                og_gla=og_gla, b_hg=b_hg, b_gla=b_gla, s_hg=s_hg, s_gla=s_gla, merged=merged, y1=y1)


def _mixer_ffn(x_lat, p, o_list, modx, norms, onorms, w_br_hg, w_br_gla, w_out, w_gate, w_up, w_down, target,
               n_ctx_tiles, name):
    rows, d = x_lat.shape
    dff = w_gate.shape[0]
    inv_d = 1.0 / d

    def body(x_ref, ofw_hg, obw_hg, ofw_gla, obw_gla, p_hgate, p_ggate, p_ghg_a, p_ghg_b, p_ggla_a, p_ggla_b,
             modx_ref, norm_ref, on_ref, wbh_ref, wbg_ref, wout_ref, wg_ref, wu_ref, wd_ref, t_ref,
             loss_ref, dz2_ref, y1_ref, mrg_ref, oghg_ref, oggla_ref, h2_ref, a_ref, du_ref, dv_ref, dy2_ref,
             stat_ref):
        i = pl.program_id(0)
        post1, pre2, post2 = norm_ref[1:2, :], norm_ref[2:3, :], norm_ref[3:4, :]
        gate1, shift2, scale2, gate2 = modx_ref[2:3, :], modx_ref[3:4, :], modx_ref[4:5, :], modx_ref[5:6, :]
        p_gate_hg = jnp.concatenate([p_ghg_a[...], p_ghg_b[...]], axis=1)
        p_gate_gla = jnp.concatenate([p_ggla_a[...], p_ggla_b[...]], axis=1)
        t = _mixer_tail(x_ref[...], ofw_hg[...] + obw_hg[...], ofw_gla[...] + obw_gla[...], p_hgate[...],
                        p_ggate[...], p_gate_hg, p_gate_gla, on_ref[0:1, 0:HD], on_ref[1:2, 0:HD],
                        wbh_ref[...], wbg_ref[...], wout_ref[...])
        y1_ref[...] = t["y1"]
        mrg_ref[...] = t["merged"]
        oghg_ref[...] = t["og_hg"]
        oggla_ref[...] = t["og_gla"]
        n1, _ = _rms(t["y1"])
        z2 = x_ref[...] + n1 * post1 * gate1
        n2, r2 = _rms(z2)
        nw2 = n2 * pre2
        h2 = (nw2 * (1.0 + scale2) + shift2).astype(BF16)
        h2_ref[...] = h2
        u = _dot_nt(h2, wg_ref[...])
        v = _dot_nt(h2, wu_ref[...])
        su = _silu(u)
        a = (su * v).astype(BF16)
        a_ref[...] = a
        y2 = jnp.dot(a, wd_ref[...], preferred_element_type=F32)
        n3, r3 = _rms(y2)
        z3 = z2 + n3 * post2 * gate2
        err = z3 - t_ref[...]
        part = 0.5 * inv_d * jnp.sum(err * err)
        dz3 = err * inv_d
        dgate2 = _colsum(dz3 * n3 * post2)
        tt = dz3 * gate2
        dpost2 = _colsum(tt * n3)
        dy2 = _rms_bwd(tt * post2, n3, r3).astype(BF16)
        dy2_ref[...] = dy2
        da = _dot_nt(dy2, wd_ref[...])
        du = (da * v * _dsilu(u)).astype(BF16)
        dv = (da * su).astype(BF16)
        du_ref[...] = du
        dv_ref[...] = dv
        dh2 = (jnp.dot(du, wg_ref[...], preferred_element_type=F32)
               + jnp.dot(dv, wu_ref[...], preferred_element_type=F32))
        dshift2 = _colsum(dh2)
        dscale2 = _colsum(dh2 * nw2)
        dnw2 = dh2 * (1.0 + scale2)
        dpre2 = _colsum(dnw2 * n2)
        dz2_ref[...] = dz3 + _rms_bwd(dnw2 * pre2, n2, r2)

        @pl.when(i == 0)
        def _():
            stat_ref[...] = jnp.zeros_like(stat_ref)
            loss_ref[...] = jnp.zeros_like(loss_ref)

        for r, val in enumerate((dshift2, dscale2, dgate2, dpre2, dpost2)):
            stat_ref[r:r + 1, :] += val
        loss_ref[...] += part
        stat_ref[5:6, 0:128] += part

    tm = TM_FFN
    ctx_tiles = n_ctx_tiles * (TM // tm)
    lat = lambda i: (i, 0)
    full = lambda i: (i + ctx_tiles, 0)
    fixed = lambda i: (0, 0)

    def pcol(blk):
        return pl.BlockSpec((tm, HW), lambda i: (i + ctx_tiles, blk))

    in_specs = ([pl.BlockSpec((tm, d), lat)] + [pl.BlockSpec((tm, HW), full)] * 4
                + [pcol(C_HGATE), pcol(C_GGATE), pcol(9), pcol(10), pcol(11), pcol(12)]
                + [pl.BlockSpec((8, d), fixed), pl.BlockSpec((8, d), fixed), pl.BlockSpec((8, d), fixed)]
                + [VMEM_SPEC] * 6 + [pl.BlockSpec((tm, d), lat)])
    bf = lambda w: jax.ShapeDtypeStruct((rows, w), BF16)
    out_shape = [jax.ShapeDtypeStruct((8, 128), F32), jax.ShapeDtypeStruct((rows, d), F32),
                 jax.ShapeDtypeStruct((rows, d), F32), bf(d), bf(HW), bf(HW), bf(d), bf(dff), bf(dff), bf(dff), bf(d),
                 jax.ShapeDtypeStruct((8, d), F32)]
    out_specs = [pl.BlockSpec((8, 128), fixed), pl.BlockSpec((tm, d), lat), pl.BlockSpec((tm, d), lat),
                 pl.BlockSpec((tm, d), lat), pl.BlockSpec((tm, HW), lat), pl.BlockSpec((tm, HW), lat),
                 pl.BlockSpec((tm, d), lat), pl.BlockSpec((tm, dff), lat), pl.BlockSpec((tm, dff), lat),
                 pl.BlockSpec((tm, dff), lat), pl.BlockSpec((tm, d), lat), pl.BlockSpec((8, d), fixed)]
    return pl.pallas_call(
        body, name=name, grid=(rows // tm,), out_shape=out_shape, in_specs=in_specs, out_specs=out_specs,
        compiler_params=_cparams(dimension_semantics=("arbitrary",)),
    )(x_lat, *o_list, p, p, p, p, p, p, modx, norms, onorms, w_br_hg, w_br_gla, w_out, w_gate, w_up, w_down, target)


def _mixer_tail_fwd(x_lat, p, o_list, modx, norms, onorms, w_br_hg, w_br_gla, w_out, n_ctx_tiles, name):
    rows, d = x_lat.shape

    def body(x_ref, ofw_hg, obw_hg, ofw_gla, obw_gla, p_hgate, p_ggate, p_ghg_a, p_ghg_b, p_ggla_a, p_ggla_b,
             modx_ref, norm_ref, on_ref, wbh_ref, wbg_ref, wout_ref, z2_ref, y1_ref, mrg_ref, oghg_ref, oggla_ref):
        p_gate_hg = jnp.concatenate([p_ghg_a[...], p_ghg_b[...]], axis=1)
        p_gate_gla = jnp.concatenate([p_ggla_a[...], p_ggla_b[...]], axis=1)
        t = _mixer_tail(x_ref[...], ofw_hg[...] + obw_hg[...], ofw_gla[...] + obw_gla[...], p_hgate[...],
                        p_ggate[...], p_gate_hg, p_gate_gla, on_ref[0:1, 0:HD], on_ref[1:2, 0:HD],
                        wbh_ref[...], wbg_ref[...], wout_ref[...])
        y1_ref[...] = t["y1"]
        mrg_ref[...] = t["merged"]
        oghg_ref[...] = t["og_hg"]
        oggla_ref[...] = t["og_gla"]
        n1, _ = _rms(t["y1"])
        z2_ref[...] = x_ref[...] + n1 * norm_ref[1:2, :] * modx_ref[2:3, :]

    lat = lambda i: (i, 0)
    full = lambda i: (i + n_ctx_tiles, 0)
    fixed = lambda i: (0, 0)

    def pcol(blk):
        return pl.BlockSpec((TM, HW), lambda i: (i + n_ctx_tiles, blk))

    in_specs = ([pl.BlockSpec((TM, d), lat)] + [pl.BlockSpec((TM, HW), full)] * 4
                + [pcol(C_HGATE), pcol(C_GGATE), pcol(9), pcol(10), pcol(11), pcol(12)]
                + [pl.BlockSpec((8, d), fixed)] * 3 + [VMEM_SPEC] * 3)
    bf = lambda w: jax.ShapeDtypeStruct((rows, w), BF16)
    f32 = jax.ShapeDtypeStruct((rows, d), F32)
    return pl.pallas_call(
        body, name=name, grid=(rows // TM,), out_shape=[f32, f32, bf(d), bf(HW), bf(HW)], in_specs=in_specs,
        out_specs=[pl.BlockSpec((TM, d), lat)] * 3 + [pl.BlockSpec((TM, HW), lat)] * 2,
        compiler_params=_cparams(dimension_semantics=("parallel",)),
    )(x_lat, *o_list, p, p, p, p, p, p, modx, norms, onorms, w_br_hg, w_br_gla, w_out)


def _ffn_fwd_bwd(z2, modx, norms, w_gate, w_up, w_down, target, name):
    rows, d = z2.shape
    dff = w_gate.shape[0]
    inv_d = 1.0 / d

    def body(z2_ref, modx_ref, norm_ref, wg_ref, wu_ref, wd_ref, t_ref,
             loss_ref, dz2_ref, h2_ref, a_ref, du_ref, dv_ref, dy2_ref, stat_ref):
        i = pl.program_id(0)
        pre2, post2 = norm_ref[2:3, :], norm_ref[3:4, :]
        shift2, scale2, gate2 = modx_ref[3:4, :], modx_ref[4:5, :], modx_ref[5:6, :]
        z2 = z2_ref[...]
        n2, r2 = _rms(z2)
        nw2 = n2 * pre2
        h2 = (nw2 * (1.0 + scale2) + shift2).astype(BF16)
        h2_ref[...] = h2
        u = _dot_nt(h2, wg_ref[...])
        v = _dot_nt(h2, wu_ref[...])
        su = _silu(u)
        a = (su * v).astype(BF16)
        a_ref[...] = a
        y2 = jnp.dot(a, wd_ref[...], preferred_element_type=F32)
        n3, r3 = _rms(y2)
        err = z2 + n3 * post2 * gate2 - t_ref[...]
        part = 0.5 * inv_d * jnp.sum(err * err)
        dz3 = err * inv_d
        dgate2 = _colsum(dz3 * n3 * post2)
        tt = dz3 * gate2
        dpost2 = _colsum(tt * n3)
        dy2 = _rms_bwd(tt * post2, n3, r3).astype(BF16)
        dy2_ref[...] = dy2
        da = _dot_nt(dy2, wd_ref[...])
        du = (da * v * _dsilu(u)).astype(BF16)
        dv = (da * su).astype(BF16)
        du_ref[...] = du
        dv_ref[...] = dv
        dh2 = (jnp.dot(du, wg_ref[...], preferred_element_type=F32)
               + jnp.dot(dv, wu_ref[...], preferred_element_type=F32))
        dshift2 = _colsum(dh2)
        dscale2 = _colsum(dh2 * nw2)
        dnw2 = dh2 * (1.0 + scale2)
        dpre2 = _colsum(dnw2 * n2)
        dz2_ref[...] = dz3 + _rms_bwd(dnw2 * pre2, n2, r2)

        @pl.when(i == 0)
        def _():
            stat_ref[...] = jnp.zeros_like(stat_ref)
            loss_ref[...] = jnp.zeros_like(loss_ref)

        for r, val in enumerate((dshift2, dscale2, dgate2, dpre2, dpost2)):
            stat_ref[r:r + 1, :] += val
        loss_ref[...] += part
        stat_ref[5:6, 0:128] += part

    lat = lambda i: (i, 0)
    fixed = lambda i: (0, 0)
    bf = lambda w: jax.ShapeDtypeStruct((rows, w), BF16)
    return pl.pallas_call(
        body, name=name, grid=(rows // TM,),
        out_shape=[jax.ShapeDtypeStruct((8, 128), F32), jax.ShapeDtypeStruct((rows, d), F32), bf(d), bf(dff), bf(dff),
                   bf(dff), bf(d), jax.ShapeDtypeStruct((8, d), F32)],
        in_specs=[pl.BlockSpec((TM, d), lat), pl.BlockSpec((8, d), fixed), pl.BlockSpec((8, d), fixed)]
        + [VMEM_SPEC] * 3 + [pl.BlockSpec((TM, d), lat)],
        out_specs=[pl.BlockSpec((8, 128), fixed), pl.BlockSpec((TM, d), lat), pl.BlockSpec((TM, d), lat),
                   pl.BlockSpec((TM, dff), lat), pl.BlockSpec((TM, dff), lat), pl.BlockSpec((TM, dff), lat),
                   pl.BlockSpec((TM, d), lat), pl.BlockSpec((8, d), fixed)],
        compiler_params=_cparams(dimension_semantics=("arbitrary",)),
    )(z2, modx, norms, w_gate, w_up, w_down, target)


def _mixer_tail_bwd(x_lat, p, o_list, dz2, y1, modx, norms, onorms, w_br_hg, w_br_gla, w_out, n_ctx_tiles, n_tiles,
                    name):
    rows, d = x_lat.shape
    total = n_tiles * TM

    def body(x_ref, ofw_hg, obw_hg, ofw_gla, obw_gla, p_hgate, p_ggate, p_ghg_a, p_ghg_b, p_ggla_a, p_ggla_b,
             dz2_ref, y1_ref, modx_ref, norm_ref, on_ref, wbh_ref, wbg_ref, wout_ref,
             dohg_ref, dogla_ref, dhgate_ref, dggate_ref, dghg_ref, dggla_ref, dy1_ref, dbhg_ref, dbgla_ref,
             stat_ref):
        i = pl.program_id(0)

        @pl.when(i == 0)
        def _():
            stat_ref[...] = jnp.zeros_like(stat_ref)

        @pl.when(i < n_ctx_tiles)
        def _():
            for ref in (dohg_ref, dogla_ref, dhgate_ref, dggate_ref, dghg_ref, dggla_ref):
                ref[...] = jnp.zeros_like(ref)

        @pl.when(i >= n_ctx_tiles)
        def _():
            post1, gate1 = norm_ref[1:2, :], modx_ref[2:3, :]
            hg_on, gla_on = on_ref[0:1, 0:HD], on_ref[1:2, 0:HD]
            p_gate_hg = jnp.concatenate([p_ghg_a[...], p_ghg_b[...]], axis=1)
            p_gate_gla = jnp.concatenate([p_ggla_a[...], p_ggla_b[...]], axis=1)
            ph, pg = p_hgate[...], p_ggate[...]
            t = _mixer_tail(x_ref[...], ofw_hg[...] + obw_hg[...], ofw_gla[...] + obw_gla[...], ph, pg,
                            p_gate_hg, p_gate_gla, hg_on, gla_on, wbh_ref[...], wbg_ref[...], wout_ref[...])
            dz2 = dz2_ref[...]
            n1, r1 = _rms(y1_ref[...])
            dgate1 = _colsum(dz2 * n1 * post1)
            tt = dz2 * gate1
            dpost1 = _colsum(tt * n1)
            dy1 = _rms_bwd(tt * post1, n1, r1).astype(BF16)
            dy1_ref[...] = dy1
            dmerged = _dot_nt(dy1, wout_ref[...])
            dghg_ref[...] = dmerged * t["b_hg"] * t["s_hg"] * (1.0 - t["s_hg"])
            dggla_ref[...] = dmerged * t["b_gla"] * t["s_gla"] * (1.0 - t["s_gla"])
            db_hg = (dmerged * t["s_hg"]).astype(BF16)
            db_gla = (dmerged * t["s_gla"]).astype(BF16)
            dbhg_ref[...] = db_hg
            dbgla_ref[...] = db_gla
            don_acc = []
            for (db, wb, pgate, on, ns, rs, gain, gate_ref, do_ref) in (
                    (db_hg, wbh_ref, ph, t["on_hg"], t["n_hg"], t["r_hg"], hg_on, dhgate_ref, dohg_ref),
                    (db_gla, wbg_ref, pg, t["on_gla"], t["n_gla"], t["r_gla"], gla_on, dggate_ref, dogla_ref)):
                dog = _dot_nt(db, wb[...])
                gate_ref[...] = dog * on * _dsilu(pgate)
                don = dog * _silu(pgate)
                acc = jnp.zeros((1, HD), F32)
                for h in range(NH):
                    sl = slice(h * HD, (h + 1) * HD)
                    acc = acc + _colsum(don[:, sl] * ns[h])
                    do_ref[:, sl] = _rms_bwd(don[:, sl] * gain, ns[h], rs[h])
                don_acc.append(acc)
            stat_ref[0:1, :] += dgate1
            stat_ref[1:2, :] += dpost1
            stat_ref[2:3, 0:HD] += don_acc[0]
            stat_ref[2:3, HD:2 * HD] += don_acc[1]

    lat = lambda i: (jnp.maximum(i - n_ctx_tiles, 0), 0)
    full = lambda i: (i, 0)
    fixed = lambda i: (0, 0)

    def pcol(blk):
        return pl.BlockSpec((TM, HW), lambda i: (i, blk))

    in_specs = ([pl.BlockSpec((TM, d), lat)] + [pl.BlockSpec((TM, HW), full)] * 4
                + [pcol(C_HGATE), pcol(C_GGATE), pcol(9), pcol(10), pcol(11), pcol(12)]
                + [pl.BlockSpec((TM, d), lat), pl.BlockSpec((TM, d), lat)]
                + [pl.BlockSpec((8, d), fixed)] * 3 + [VMEM_SPEC] * 3)
    f = lambda w: jax.ShapeDtypeStruct((total, w), F32)
    out_shape = [f(HW), f(HW), f(HW), f(HW), f(d), f(d), jax.ShapeDtypeStruct((rows, d), BF16),
                 jax.ShapeDtypeStruct((rows, d), BF16), jax.ShapeDtypeStruct((rows, d), BF16),
                 jax.ShapeDtypeStruct((8, d), F32)]
    out_specs = ([pl.BlockSpec((TM, HW), full)] * 4 + [pl.BlockSpec((TM, d), full)] * 2
                 + [pl.BlockSpec((TM, d), lat)] * 3 + [pl.BlockSpec((8, d), fixed)])
    return pl.pallas_call(
        body, name=name, grid=(n_tiles,), out_shape=out_shape, in_specs=in_specs, out_specs=out_specs,
        compiler_params=_cparams(dimension_semantics=("arbitrary",)),
    )(x_lat, *o_list, p, p, p, p, p, p, dz2, y1, modx, norms, onorms, w_br_hg, w_br_gla, w_out)


def _in_projection_bwd(ctx0, x0, dz2, modc, modx, pre1, w_t, pieces, n_ctx_tiles, name):
    d = x0.shape[1]
    rows = ctx0.shape[0] + x0.shape[0]
    lat_rows = dz2.shape[0]
    width = P_WIDTH
    n_pieces = len(pieces)

    def body(*refs):
        ctx_ref, x_ref, dz2_ref, modc_ref, modx_ref, pre_ref, w_ref = refs[:7]
        (dhq_f, dhq_b, dhi_f, dhi_b, dhf_f, dhf_b, dhgate, dgq_f, dgq_b, dgk_f, dgk_b, dgv_f, dgv_b, dggate,
         dghg, dggla, dlr_f, dlr_b) = refs[7:7 + n_pieces]
        dp_ref, gx_ref, stat_ref = refs[7 + n_pieces:]
        i = pl.program_id(0)
        is_ctx = i < n_ctx_tiles
        z = jnp.where(is_ctx, ctx_ref[...], x_ref[...])
        sections = [
            (0, dhq_f[...] + dhq_b[...]), (HW, dhi_f[...] + dhi_b[...]), (2 * HW, dhf_f[...]), (3 * HW, dhf_b[...]),
            (4 * HW, dhgate[...]), (5 * HW, dgq_f[...] + dgq_b[...]), (6 * HW, dgk_f[...] + dgk_b[...]),
            (7 * HW, dgv_f[...] + dgv_b[...]), (8 * HW, dggate[...]),
            (9 * HW, dghg[:, 0:HW]), (10 * HW, dghg[:, HW:2 * HW]),
            (11 * HW, dggla[:, 0:HW]), (12 * HW, dggla[:, HW:2 * HW]), (OFF_LR, dlr_f[...] + dlr_b[...])]
        dh = jnp.zeros((TM, d), F32)
        for off, val in sections:
            w = val.shape[1]
            vb = val.astype(BF16)
            dp_ref[:, off:off + w] = vb
            dh = dh + jnp.dot(vb, w_ref[_w_in_row(off):_w_in_row(off) + w, :], preferred_element_type=F32)
        n, r = _rms(z)
        pre = pre_ref[...]
        scale = jnp.where(is_ctx, modc_ref[1:2, :], modx_ref[1:2, :])
        nw = n * pre
        dshift = _colsum(dh)
        dscale = _colsum(dh * nw)
        dnw = dh * (1.0 + scale)
        dpre = _colsum(dnw * n)
        gx_ref[...] = dz2_ref[...] + _rms_bwd(dnw * pre, n, r)
        zero = jnp.zeros((1, d), F32)

        @pl.when(i == 0)
        def _():
            stat_ref[...] = jnp.zeros_like(stat_ref)

        stat_ref[0:1, :] += jnp.where(is_ctx, zero, dshift)
        stat_ref[1:2, :] += jnp.where(is_ctx, zero, dscale)
        stat_ref[2:3, :] += jnp.where(is_ctx, dshift, zero)
        stat_ref[3:4, :] += jnp.where(is_ctx, dscale, zero)
        stat_ref[4:5, :] += dpre

    full = lambda i: (i, 0)
    lat = lambda i: (jnp.maximum(i - n_ctx_tiles, 0), 0)
    fixed = lambda i: (0, 0)
    piece_specs = [pl.BlockSpec((TM, a.shape[1]), full) for a in pieces]
    in_specs = [pl.BlockSpec((TM, d), lambda i: (jnp.minimum(i, n_ctx_tiles - 1), 0)), pl.BlockSpec((TM, d), lat),
                pl.BlockSpec((TM, d), lat), pl.BlockSpec((8, d), fixed),
                pl.BlockSpec((8, d), fixed), pl.BlockSpec((1, d), fixed), VMEM_SPEC] + piece_specs
    return pl.pallas_call(
        body, name=name, grid=(rows // TM,),
        out_shape=[jax.ShapeDtypeStruct((rows, width), BF16), jax.ShapeDtypeStruct((lat_rows, d), F32),
                   jax.ShapeDtypeStruct((8, d), F32)],
        in_specs=in_specs,
        out_specs=[pl.BlockSpec((TM, width), full), pl.BlockSpec((TM, d), lat), pl.BlockSpec((8, d), fixed)],
        compiler_params=_cparams(dimension_semantics=("arbitrary",)),
    )(ctx0, x0, dz2, modc, modx, pre1, w_t, *pieces)


def _transposed_lhs_matmul(x_ref, dy_ref, o_ref, xt_ref):
    @pl.when(pl.program_id(1) == 0)
    def _():
        xt_ref[...] = x_ref[...].T

    o_ref[...] = jnp.dot(xt_ref[...], dy_ref[...], preferred_element_type=F32)


def _w_in_grad(dp, h1, n_cols, name):
    rows, d = h1.shape
    n_main = OFF_LR // HW
    lr0 = _w_in_row(OFF_LR)

    def body(x_ref, xlr_ref, h_ref, o_hbm, xt_ref, acc_ref, sem):
        i = pl.program_id(0)

        def main_copy(step):
            row = jnp.where(step < 9, step * HW, step * HW + 2 * RANK)
            return pltpu.make_async_copy(acc_ref, o_hbm.at[pl.ds(pl.multiple_of(row, 8), HW), :], sem)

        lr_copy = pltpu.make_async_copy(acc_ref.at[0:2 * RANK, :], o_hbm.at[lr0:lr0 + 2 * RANK, :], sem)

        @pl.when(i < n_main)
        def _():
            xt_ref[...] = x_ref[...].T

        @pl.when(i > 0)
        def _():
            main_copy(i - 1).wait()

        @pl.when(i < n_main)
        def _():
            acc_ref[...] = jnp.dot(xt_ref[...], h_ref[...], preferred_element_type=F32)
            main_copy(i).start()

        @pl.when(i == n_main)
        def _():
            xt_ref[0:128, :] = xlr_ref[...].T
            acc_ref[0:128, :] = jnp.dot(xt_ref[0:128, :], h_ref[...], preferred_element_type=F32)
            lr_copy.start()
            lr_copy.wait()

    return pl.pallas_call(
        body, name=name, grid=(n_main + 1,),
        out_shape=jax.ShapeDtypeStruct((n_cols, d), F32),
        in_specs=[pl.BlockSpec((rows, HW), lambda i: (0, jnp.minimum(i, n_main - 1))),
                  pl.BlockSpec((rows, 128), lambda i: (0, OFF_LR // 128)),
                  pl.BlockSpec((rows, d), lambda i: (0, 0))],
        out_specs=ANY_SPEC,
        scratch_shapes=[pltpu.VMEM((HW, rows), BF16), pltpu.VMEM((HW, d), F32), pltpu.SemaphoreType.DMA],
        compiler_params=_cparams(dimension_semantics=("arbitrary",)),
    )(dp, dp, h1)


def _weight_grad(xs, dy, name, tk=None, tn=512, k_first=0, k_tiles=None):
    rows = dy.shape[0]
    n = dy.shape[1]
    tn_ = min(tn, n)
    tk_ = xs.shape[1] if tk is None else tk
    k_tiles = xs.shape[1] // tk_ if k_tiles is None else k_tiles
    k = k_tiles * tk_

    return pl.pallas_call(
        functools.partial(_transposed_lhs_matmul), name=name, grid=(k_tiles, n // tn_),
        out_shape=jax.ShapeDtypeStruct((k, n), F32),
        in_specs=[pl.BlockSpec((rows, tk_), lambda i, j: (0, i + k_first)),
                  pl.BlockSpec((rows, tn_), lambda i, j: (0, j))],
        out_specs=pl.BlockSpec((tk_, tn_), lambda i, j: (i, j)),
        scratch_shapes=[pltpu.VMEM((tk_, rows), BF16)],
        compiler_params=_cparams(dimension_semantics=("parallel", "arbitrary")),
    )(xs, dy)


def _running_sum(x, fw):
    c = x.shape[0]
    row = lax.broadcasted_iota(jnp.int32, (c, 1), 0)
    s = 1
    while s < c:
        if fw:
            x = x + jnp.where(row >= s, pltpu.roll(x, s, axis=0), 0.0)
        else:
            x = x + jnp.where(row < c - s, pltpu.roll(x, c - s, axis=0), 0.0)
        s *= 2
    return x


def _chunk_terms(q, k, g, fw):
    c = CHUNK
    r = lax.broadcasted_iota(jnp.int32, (c, c), 0)
    s = lax.broadcasted_iota(jnp.int32, (c, c), 1)
    causal = (s <= r) if fw else (s >= r)
    causal_t = (s >= r) if fw else (s <= r)
    cum = _running_sum(g, fw)
    row = lax.broadcasted_iota(jnp.int32, (c, 1), 0)
    pos = row if fw else (c - 1 - row)
    starts = [None]
    for j in range(1, NSUB):
        rj = SUB * j - 1 if fw else c - SUB * j
        starts.append(cum[rj:rj + 1, :])
    in_blk = [(pos >= SUB * j) & (pos < SUB * (j + 1)) for j in range(NSUB)]
    e = [jnp.exp(cum)]
    for j in range(1, NSUB):
        e.append(jnp.exp(jnp.where(pos >= SUB * j, cum - starts[j], -1e30)))
    own = jnp.zeros_like(cum)
    for j in range(1, NSUB):
        own = own + jnp.where(in_blk[j], starts[j], 0.0)
    kscale = jnp.exp(own - cum)
    rend = c - 1 if fw else 0
    cend = cum[rend:rend + 1, :]
    tail = jnp.exp(cend - cum)
    qcat = jnp.concatenate([q * e[j] for j in range(NSUB)], axis=1).astype(BF16)
    kt = k * kscale
    km = jnp.concatenate([jnp.where(in_blk[j], kt, 0.0) for j in range(NSUB)], axis=1).astype(BF16)
    return dict(causal=causal, causal_t=causal_t, e=e, in_blk=in_blk, kscale=kscale, cend=cend, tail=tail,
                qcat=qcat, km=km)


def _chunk_fwd(q, k, v, g, st0, fw):
    t = _chunk_terms(q, k, g, fw)
    a = jnp.where(t["causal"], _dot_nt(t["qcat"], t["km"]), 0.0)
    o = _dot(a, v) + _dot_nt(t["qcat"][:, 0:HD], st0)
    st1 = st0 * jnp.exp(t["cend"]) + _dot_tn(v, k * t["tail"])
    return o, st1


def _chunk_bwd(q, k, v, g, st0, do, dst1, fw):
    t = _chunk_terms(q, k, g, fw)
    qcat, km, e = t["qcat"], t["km"], t["e"]
    a_t = jnp.where(t["causal_t"], _dot_nt(km, qcat), 0.0)
    ktail = k * t["tail"]
    dv = _dot(a_t, do) + _dot_nt(ktail, dst1)
    da = jnp.where(t["causal"], _dot_nt(do, v), 0.0)
    da_t = jnp.where(t["causal_t"], _dot_nt(v, do), 0.0)
    dqcat = _dot(da, km)
    dq_inter = e[0] * _dot(do, st0)
    dq = dq_inter
    for j in range(NSUB):
        dq = dq + e[j] * dqcat[:, j * HD:(j + 1) * HD]
    dkm = _dot(da_t, qcat)
    dkt = jnp.zeros_like(k)
    for j in range(NSUB):
        dkt = dkt + jnp.where(t["in_blk"][j], dkm[:, j * HD:(j + 1) * HD], 0.0)
    dk_inter = _dot(v, dst1) * t["tail"]
    dk = dkt * t["kscale"] + dk_inter
    dcum = q * dq_inter - k * dk_inter
    for j in range(NSUB):
        sl = slice(j * HD, (j + 1) * HD)
        dcum = dcum + qcat[:, sl].astype(F32) * dqcat[:, sl] - km[:, sl].astype(F32) * dkm[:, sl]
    ecend = jnp.exp(t["cend"])
    end = ecend * _colsum(st0 * dst1) + _colsum(k * dk_inter)
    dg = _running_sum(dcum, not fw) + end
    dst0 = dst1 * ecend + _dot_tn(do, q * e[0])
    return dq, dk, dv, dg, dst0


def _running_sums(xs, fws):
    c = xs[0].shape[0]
    row = lax.broadcasted_iota(jnp.int32, (c, 1), 0)
    s = 1
    while s < c:
        xs = [x + (jnp.where(row >= s, pltpu.roll(x, s, axis=0), 0.0) if fw else
                   jnp.where(row < c - s, pltpu.roll(x, c - s, axis=0), 0.0)) for x, fw in zip(xs, fws)]
        s *= 2
    return xs


def _chunks_terms(qs, ks, gs, fws):
    c = CHUNK
    n = len(qs)
    r = lax.broadcasted_iota(jnp.int32, (c, c), 0)
    s = lax.broadcasted_iota(jnp.int32, (c, c), 1)
    row = lax.broadcasted_iota(jnp.int32, (c, 1), 0)
    per_dir = {}
    for fw in set(fws):
        pos = row if fw else (c - 1 - row)
        per_dir[fw] = dict(
            causal=(s <= r) if fw else (s >= r), causal_t=(s >= r) if fw else (s <= r), pos=pos,
            in_blk=[(pos >= SUB * j) & (pos < SUB * (j + 1)) for j in range(NSUB)],
            start_row=[None] + [SUB * j - 1 if fw else c - SUB * j for j in range(1, NSUB)],
            rend=c - 1 if fw else 0)
    dirs = [per_dir[fw] for fw in fws]
    cums = _running_sums(gs, fws)
    starts = [[None] + [cum[d["start_row"][j]:d["start_row"][j] + 1, :] for j in range(1, NSUB)]
              for cum, d in zip(cums, dirs)]
    es = [[jnp.exp(cum) for cum in cums]]
    for j in range(1, NSUB):
        es.append([jnp.exp(jnp.where(d["pos"] >= SUB * j, cum - st[j], -1e30)) for cum, st, d in zip(cums, starts, dirs)])
    owns = [sum(jnp.where(d["in_blk"][j], st[j], 0.0) for j in range(1, NSUB)) for st, d in zip(starts, dirs)]
    kscales = [jnp.exp(own - cum) for own, cum in zip(owns, cums)]
    cends = [cum[d["rend"]:d["rend"] + 1, :] for cum, d in zip(cums, dirs)]
    tails = [jnp.exp(cend - cum) for cend, cum in zip(cends, cums)]
    qcats = [jnp.concatenate([q * es[j][i] for j in range(NSUB)], axis=1).astype(BF16) for i, q in enumerate(qs)]
    kts = [k * ksc for k, ksc in zip(ks, kscales)]
    kms = [jnp.concatenate([jnp.where(d["in_blk"][j], kt, 0.0) for j in range(NSUB)], axis=1).astype(BF16)
           for kt, d in zip(kts, dirs)]
    e_by_lane = [[es[j][i] for j in range(NSUB)] for i in range(n)]
    return dict(dirs=dirs, e=e_by_lane, kscale=kscales, cend=cends, tail=tails, qcat=qcats, km=kms)


def _chunks_fwd(qs, ks, vs, gs, st0s, fws):
    t = _chunks_terms(qs, ks, gs, fws)
    scores = [_dot_nt(qc, km) for qc, km in zip(t["qcat"], t["km"])]
    a = [jnp.where(d["causal"], sc, 0.0) for sc, d in zip(scores, t["dirs"])]
    inter = [_dot_nt(qc[:, 0:HD], st0) for qc, st0 in zip(t["qcat"], st0s)]
    intra = [_dot(a_, v) for a_, v in zip(a, vs)]
    os_ = [x + y for x, y in zip(intra, inter)]
    upd = [_dot_tn(v, k * tl) for v, k, tl in zip(vs, ks, t["tail"])]
    st1s = [st0 * jnp.exp(ce) + u for st0, ce, u in zip(st0s, t["cend"], upd)]
    return os_, st1s


def _chunks_bwd(qs, ks, vs, gs, st0s, dos, dst1s, fws):
    n = len(qs)
    t = _chunks_terms(qs, ks, gs, fws)
    qcat, km, e, dirs = t["qcat"], t["km"], t["e"], t["dirs"]
    a_t = [jnp.where(d["causal_t"], _dot_nt(km_, qc), 0.0) for km_, qc, d in zip(km, qcat, dirs)]
    ktail = [k * tl for k, tl in zip(ks, t["tail"])]
    dv_a = [_dot(at, do) for at, do in zip(a_t, dos)]
    dv_b = [_dot_nt(kt, ds) for kt, ds in zip(ktail, dst1s)]
    dv = [x + y for x, y in zip(dv_a, dv_b)]
    da = [jnp.where(d["causal"], _dot_nt(do, v), 0.0) for do, v, d in zip(dos, vs, dirs)]
    da_t = [jnp.where(d["causal_t"], _dot_nt(v, do), 0.0) for do, v, d in zip(dos, vs, dirs)]
    dqcat = [_dot(da_, km_) for da_, km_ in zip(da, km)]
    dq_inter = [e[i][0] * _dot(dos[i], st0s[i]) for i in range(n)]
    dkm = [_dot(dat, qc) for dat, qc in zip(da_t, qcat)]
    dk_inter = [_dot(v, ds) * tl for v, ds, tl in zip(vs, dst1s, t["tail"])]
    dq = [dq_inter[i] + sum(e[i][j] * dqcat[i][:, j * HD:(j + 1) * HD] for j in range(NSUB)) for i in range(n)]
    dkt = [sum(jnp.where(dirs[i]["in_blk"][j], dkm[i][:, j * HD:(j + 1) * HD], 0.0) for j in range(NSUB))
           for i in range(n)]
    dk = [dkt[i] * t["kscale"][i] + dk_inter[i] for i in range(n)]
    dcum = [qs[i] * dq_inter[i] - ks[i] * dk_inter[i]
            + sum(qcat[i][:, j * HD:(j + 1) * HD].astype(F32) * dqcat[i][:, j * HD:(j + 1) * HD]
                  - km[i][:, j * HD:(j + 1) * HD].astype(F32) * dkm[i][:, j * HD:(j + 1) * HD] for j in range(NSUB))
            for i in range(n)]
    ecend = [jnp.exp(ce) for ce in t["cend"]]
    end = [ecend[i] * _colsum(st0s[i] * dst1s[i]) + _colsum(ks[i] * dk_inter[i]) for i in range(n)]
    sums = _running_sums(dcum, [not fw for fw in fws])
    dg = [sm + en for sm, en in zip(sums, end)]
    upd = [_dot_tn(dos[i], qs[i] * e[i][0]) for i in range(n)]
    dst0 = [dst1s[i] * ecend[i] + upd[i] for i in range(n)]
    return dq, dk, dv, dg, dst0


def _chunk_index(step, n_ctx_chunks, n_chunks, fw):
    if fw:
        return step
    return jnp.where(step < n_ctx_chunks, n_ctx_chunks - 1 - step, n_chunks - 1 + n_ctx_chunks - step)


def _hg_inputs(hq, hf, lbv, d_idx, sl):
    lb = _sigmoid(lbv[d_idx:d_idx + 1, sl] - lbv[2 + d_idx:3 + d_idx, sl])
    sg = _sigmoid(hf)
    f = lb + (1.0 - lb) * sg
    return _silu(hq), 1.0 - f, jnp.log(f), f, sg, lb


def _scan_fwd(p, side, n_ctx_chunks, fw, branch, name):
    rows = p.shape[0]
    n_chunks = rows // CHUNK
    d_idx = 0 if fw else 1
    hg = branch == "hg"
    cols = (C_HQ, C_HI, C_HF_FW + d_idx) if hg else (C_GQ, C_GK, C_GV)

    def body(*refs):
        if hg:
            a_ref, b_ref, c_ref, lb_ref, o_ref, st_ref, state = refs
        else:
            a_ref, b_ref, c_ref, lr_ref, wgk_ref, bgk_ref, o_ref, st_ref, state = refs
            logits = _dot(lr_ref[...], wgk_ref[...]) + bgk_ref[...]
            g_all = _log_sigmoid(logits) * (1.0 / GATE_NORM)

        @pl.when(pl.program_id(0) == 0)
        def _():
            state[...] = jnp.zeros_like(state)

        for h in range(NH):
            sl = slice(h * HD, (h + 1) * HD)
            if hg:
                q, k, g, _, _, _ = _hg_inputs(a_ref[:, sl], c_ref[:, sl], lb_ref[...], d_idx, sl)
                v = b_ref[:, sl]
            else:
                q, k, v, g = a_ref[:, sl] * (HD ** -0.5), b_ref[:, sl], c_ref[:, sl], g_all[:, sl]
            st0 = state[h]
            st_ref[0, h] = st0
            o, st1 = _chunk_fwd(q, k, v, g, st0, fw)
            o_ref[:, sl] = o
            state[h] = st1

    def cmap(blk):
        return pl.BlockSpec((CHUNK, HW), lambda j: (_chunk_index(j, n_ctx_chunks, n_chunks, fw), blk))

    fixed = lambda j: (0, 0)
    in_specs = [cmap(cols[0]), cmap(cols[1]), cmap(cols[2])]
    if hg:
        in_specs += [pl.BlockSpec((4, HW), fixed)]
        args = (p, p, p, side)
    else:
        in_specs += [pl.BlockSpec((CHUNK, 128), lambda j: (_chunk_index(j, n_ctx_chunks, n_chunks, fw), OFF_LR // 128)),
                     pl.BlockSpec((128, HW), fixed), pl.BlockSpec((1, HW), fixed)]
        args = (p, p, p, p, side[0], side[1])
    return pl.pallas_call(
        body, name=name, grid=(n_chunks,),
        out_shape=[jax.ShapeDtypeStruct((rows, HW), F32), jax.ShapeDtypeStruct((n_chunks, NH, HD, HD), F32)],
        in_specs=in_specs,
        out_specs=[pl.BlockSpec((CHUNK, HW), lambda j: (_chunk_index(j, n_ctx_chunks, n_chunks, fw), 0)),
                   pl.BlockSpec((1, NH, HD, HD), lambda j: (_chunk_index(j, n_ctx_chunks, n_chunks, fw), 0, 0, 0))],
        scratch_shapes=[pltpu.VMEM((NH, HD, HD), F32)],
        compiler_params=_cparams(dimension_semantics=("arbitrary",)),
    )(*args)


def _scan_fwd_both(p, side, n_ctx_chunks, branch, name):
    rows = p.shape[0]
    n_chunks = rows // CHUNK
    hg = branch == "hg"
    n_in = 4 if hg else 6

    def body(*refs):
        ins, outs, state = refs[:2 * n_in], refs[2 * n_in:2 * n_in + 4], refs[-1]

        @pl.when(pl.program_id(0) == 0)
        def _():
            state[...] = jnp.zeros_like(state)

        lanes, where = [], []
        for di, fw in enumerate((True, False)):
            r = ins[di * n_in:(di + 1) * n_in]
            o_ref, st_ref = outs[2 * di], outs[2 * di + 1]
            if hg:
                a_ref, b_ref, c_ref, lb_ref = r
            else:
                a_ref, b_ref, c_ref, lr_ref, wgk_ref, bgk_ref = r
                logits = _dot(lr_ref[...], wgk_ref[...]) + bgk_ref[...]
                g_all = _log_sigmoid(logits) * (1.0 / GATE_NORM)
            for h in range(NH):
                sl = slice(h * HD, (h + 1) * HD)
                if hg:
                    q, k, g, _, _, _ = _hg_inputs(a_ref[:, sl], c_ref[:, sl], lb_ref[...], di, sl)
                    v = b_ref[:, sl]
                else:
                    q, k, v, g = a_ref[:, sl] * (HD ** -0.5), b_ref[:, sl], c_ref[:, sl], g_all[:, sl]
                lanes.append((q, k, v, g, state[di, h], fw))
                where.append((di, h, sl, o_ref, st_ref))
        qs, ks, vs, gs, st0s, fws = (list(col) for col in zip(*lanes))
        os_, st1s = _chunks_fwd(qs, ks, vs, gs, st0s, fws)
        for (di, h, sl, o_ref, st_ref), st0, o, st1 in zip(where, st0s, os_, st1s):
            st_ref[0, h] = st0
            o_ref[:, sl] = o
            state[di, h] = st1

    fixed = lambda j: (0, 0)
    in_specs, args, out_specs = [], [], []
    for di, fw in enumerate((True, False)):
        chunk = functools.partial(_chunk_index, n_ctx_chunks=n_ctx_chunks, n_chunks=n_chunks, fw=fw)

        def cmap(blk, width=HW, chunk=chunk):
            return pl.BlockSpec((CHUNK, width), lambda j: (chunk(j), blk))

        if hg:
            in_specs += [cmap(C_HQ), cmap(C_HI), cmap(C_HF_FW + di), pl.BlockSpec((4, HW), fixed)]
            args += [p, p, p, side]
        else:
            in_specs += [cmap(C_GQ), cmap(C_GK), cmap(C_GV), cmap(OFF_LR // 128, 128),
                         pl.BlockSpec((128, HW), fixed), pl.BlockSpec((1, HW), fixed)]
            args += [p, p, p, p, side[di][0], side[di][1]]
        out_specs += [cmap(0), pl.BlockSpec((1, NH, HD, HD), lambda j, chunk=chunk: (chunk(j), 0, 0, 0))]
    return pl.pallas_call(
        body, name=name, grid=(n_chunks,),
        out_shape=[jax.ShapeDtypeStruct((rows, HW), F32), jax.ShapeDtypeStruct((n_chunks, NH, HD, HD), F32)] * 2,
        in_specs=in_specs, out_specs=out_specs,
        scratch_shapes=[pltpu.VMEM((2, NH, HD, HD), F32)],
        compiler_params=_cparams(dimension_semantics=("arbitrary",)),
    )(*args)


def _scan_bwd_both(p, side, states, d_o, n_ctx_chunks, branch, name):
    rows = p.shape[0]
    n_chunks = rows // CHUNK
    hg = branch == "hg"
    n_in = 6 if hg else 8
    n_out = 4 if hg else 6

    def body(*refs):
        ins, outs, dstate = refs[:2 * n_in], refs[2 * n_in:2 * n_in + 2 * n_out], refs[-1]
        first = pl.program_id(0) == 0

        @pl.when(first)
        def _():
            dstate[...] = jnp.zeros_like(dstate)

        lanes, where, extra, ctx = [], [], [], []
        for di, fw in enumerate((True, False)):
            r, w = ins[di * n_in:(di + 1) * n_in], outs[di * n_out:(di + 1) * n_out]
            if hg:
                a_ref, b_ref, c_ref, lb_ref, st_ref, do_ref = r
                da_ref, db_ref, dc_ref, dlb_ref = w
                acc_refs = (dlb_ref,)
            else:
                a_ref, b_ref, c_ref, lr_ref, wgk_ref, bgk_ref, st_ref, do_ref = r
                da_ref, db_ref, dc_ref, dlr_ref, dwgk_ref, dbias_ref = w
                acc_refs = (dwgk_ref, dbias_ref)
                lr = lr_ref[...]
                logits = _dot(lr, wgk_ref[...]) + bgk_ref[...]
                g_all = _log_sigmoid(logits) * (1.0 / GATE_NORM)

            @pl.when(first)
            def _(acc_refs=acc_refs):
                for ref in acc_refs:
                    ref[...] = jnp.zeros_like(ref)

            for h in range(NH):
                sl = slice(h * HD, (h + 1) * HD)
                if hg:
                    hq, hf = a_ref[:, sl], c_ref[:, sl]
                    q, k, g, f, sg, lb = _hg_inputs(hq, hf, lb_ref[...], di, sl)
                    v = b_ref[:, sl]
                    extra.append((hq, f, sg, lb))
                else:
                    q, k, v, g = a_ref[:, sl] * (HD ** -0.5), b_ref[:, sl], c_ref[:, sl], g_all[:, sl]
                    extra.append(None)
                lanes.append((q, k, v, g, st_ref[0, h], do_ref[:, sl], dstate[di, h], fw))
                where.append((di, h, sl))
            ctx.append((w, None if hg else (lr, logits, wgk_ref)))

        qs, ks, vs, gs, st0s, dos, dst1s, fws = (list(col) for col in zip(*lanes))
        dqs, dks, dvs, dgs, dst0s = _chunks_bwd(qs, ks, vs, gs, st0s, dos, dst1s, fws)
        dg_parts = {0: [], 1: []}
        for (di, h, sl), ex, dq, dk, dv, dg, dst0 in zip(where, extra, dqs, dks, dvs, dgs, dst0s):
            dstate[di, h] = dst0
            w = ctx[di][0]
            if hg:
                hq, f, sg, lb = ex
                da_ref, db_ref, dc_ref, dlb_ref = w
                da_ref[:, sl] = dq * _dsilu(hq)
                db_ref[:, sl] = dv
                df = dg / f - dk
                dc_ref[:, sl] = df * (1.0 - lb) * sg * (1.0 - sg)
                dlb_ref[0:1, sl] += _colsum(df * (1.0 - sg))
            else:
                da_ref, db_ref, dc_ref = w[:3]
                da_ref[:, sl] = dq * (HD ** -0.5)
                db_ref[:, sl] = dk
                dc_ref[:, sl] = dv
                dg_parts[di].append(dg)
        if not hg:
            for di in range(2):
                dlr_ref, dwgk_ref, dbias_ref = ctx[di][0][3:]
                lr, logits, wgk_ref = ctx[di][1]
                dlogits = jnp.concatenate(dg_parts[di], axis=1) * (1.0 / GATE_NORM) * (1.0 - _sigmoid(logits))
                dlr_ref[...] = _dot_nt(dlogits, wgk_ref[...])
                dwgk_ref[...] += _dot_tn(lr, dlogits)
                dbias_ref[0:1, :] += _colsum(dlogits)

    fixed = lambda j: (0, 0)
    big = jax.ShapeDtypeStruct((rows, HW), F32)
    in_specs, args, out_shape, out_specs = [], [], [], []
    for di, fw in enumerate((True, False)):
        def chunk_of(j, fw=fw):
            return _chunk_index(n_chunks - 1 - j, n_ctx_chunks, n_chunks, fw)

        def cmap(blk, width=HW, chunk_of=chunk_of):
            return pl.BlockSpec((CHUNK, width), lambda j: (chunk_of(j), blk))

        st_spec = pl.BlockSpec((1, NH, HD, HD), lambda j, chunk_of=chunk_of: (chunk_of(j), 0, 0, 0))
        if hg:
            in_specs += [cmap(C_HQ), cmap(C_HI), cmap(C_HF_FW + di), pl.BlockSpec((4, HW), fixed), st_spec, cmap(0)]
            args += [p, p, p, side, states[di], d_o]
            out_shape += [big, big, big, jax.ShapeDtypeStruct((8, HW), F32)]
            out_specs += [cmap(0), cmap(0), cmap(0), pl.BlockSpec((8, HW), fixed)]
        else:
            in_specs += [cmap(C_GQ), cmap(C_GK), cmap(C_GV), cmap(OFF_LR // 128, 128),
                         pl.BlockSpec((128, HW), fixed), pl.BlockSpec((1, HW), fixed), st_spec, cmap(0)]
            args += [p, p, p, p, side[di][0], side[di][1], states[di], d_o]
            out_shape += [big, big, big, jax.ShapeDtypeStruct((rows, 128), F32),
                          jax.ShapeDtypeStruct((128, HW), F32), jax.ShapeDtypeStruct((8, HW), F32)]
            out_specs += [cmap(0), cmap(0), cmap(0), cmap(0, 128), pl.BlockSpec((128, HW), fixed),
                          pl.BlockSpec((8, HW), fixed)]
    return pl.pallas_call(
        body, name=name, grid=(n_chunks,), out_shape=out_shape, in_specs=in_specs, out_specs=out_specs,
        scratch_shapes=[pltpu.VMEM((2, NH, HD, HD), F32)],
        compiler_params=_cparams(dimension_semantics=("arbitrary",)),
    )(*args)


def _scan_bwd(p, side, states, d_o, n_ctx_chunks, fw, branch, name):
    rows = p.shape[0]
    n_chunks = rows // CHUNK
    d_idx = 0 if fw else 1
    hg = branch == "hg"
    cols = (C_HQ, C_HI, C_HF_FW + d_idx) if hg else (C_GQ, C_GK, C_GV)

    def body(*refs):
        if hg:
            a_ref, b_ref, c_ref, lb_ref, st_ref, do_ref, da_ref, db_ref, dc_ref, dlb_ref, dstate = refs
        else:
            (a_ref, b_ref, c_ref, lr_ref, wgk_ref, bgk_ref, st_ref, do_ref, da_ref, db_ref, dc_ref, dlr_ref,
             dwgk_ref, dbias_ref, dstate) = refs
            lr = lr_ref[...]
            logits = _dot(lr, wgk_ref[...]) + bgk_ref[...]
            g_all = _log_sigmoid(logits) * (1.0 / GATE_NORM)

        @pl.when(pl.program_id(0) == 0)
        def _():
            dstate[...] = jnp.zeros_like(dstate)
            if hg:
                dlb_ref[...] = jnp.zeros_like(dlb_ref)
            else:
                dwgk_ref[...] = jnp.zeros_like(dwgk_ref)
                dbias_ref[...] = jnp.zeros_like(dbias_ref)

        dg_parts = []
        for h in range(NH):
            sl = slice(h * HD, (h + 1) * HD)
            if hg:
                hq, hf = a_ref[:, sl], c_ref[:, sl]
                q, k, g, f, sg, lb = _hg_inputs(hq, hf, lb_ref[...], d_idx, sl)
                v = b_ref[:, sl]
            else:
                q, k, v, g = a_ref[:, sl] * (HD ** -0.5), b_ref[:, sl], c_ref[:, sl], g_all[:, sl]
            dq, dk, dv, dg, dst0 = _chunk_bwd(q, k, v, g, st_ref[0, h], do_ref[:, sl], dstate[h], fw)
            dstate[h] = dst0
            if hg:
                da_ref[:, sl] = dq * _dsilu(hq)
                db_ref[:, sl] = dv
                df = dg / f - dk
                dc_ref[:, sl] = df * (1.0 - lb) * sg * (1.0 - sg)
                dlb_ref[0:1, sl] += _colsum(df * (1.0 - sg))
            else:
                da_ref[:, sl] = dq * (HD ** -0.5)
                db_ref[:, sl] = dk
                dc_ref[:, sl] = dv
                dg_parts.append(dg)
        if not hg:
            dlogits = jnp.concatenate(dg_parts, axis=1) * (1.0 / GATE_NORM) * (1.0 - _sigmoid(logits))
            dlr_ref[...] = _dot_nt(dlogits, wgk_ref[...])
            dwgk_ref[...] += _dot_tn(lr, dlogits)
            dbias_ref[0:1, :] += _colsum(dlogits)

    def chunk_of(j):
        return _chunk_index(n_chunks - 1 - j, n_ctx_chunks, n_chunks, fw)

    def cmap(blk, width=HW):
        return pl.BlockSpec((CHUNK, width), lambda j: (chunk_of(j), blk))

    fixed = lambda j: (0, 0)
    st_spec = pl.BlockSpec((1, NH, HD, HD), lambda j: (chunk_of(j), 0, 0, 0))
    big = jax.ShapeDtypeStruct((rows, HW), F32)
    if hg:
        in_specs = [cmap(cols[0]), cmap(cols[1]), cmap(cols[2]), pl.BlockSpec((4, HW), fixed), st_spec, cmap(0)]
        args = (p, p, p, side, states, d_o)
        out_shape = [big, big, big, jax.ShapeDtypeStruct((8, HW), F32)]
        out_specs = [cmap(0), cmap(0), cmap(0), pl.BlockSpec((8, HW), fixed)]
    else:
        in_specs = [cmap(cols[0]), cmap(cols[1]), cmap(cols[2]), cmap(OFF_LR // 128, 128),
                    pl.BlockSpec((128, HW), fixed), pl.BlockSpec((1, HW), fixed), st_spec, cmap(0)]
        args = (p, p, p, p, side[0], side[1], states, d_o)
        out_shape = [big, big, big, jax.ShapeDtypeStruct((rows, 128), F32), jax.ShapeDtypeStruct((128, HW), F32),
                     jax.ShapeDtypeStruct((8, HW), F32)]
        out_specs = [cmap(0), cmap(0), cmap(0), cmap(0, 128), pl.BlockSpec((128, HW), fixed),
                     pl.BlockSpec((8, HW), fixed)]
    return pl.pallas_call(
        body, name=name, grid=(n_chunks,), out_shape=out_shape, in_specs=in_specs, out_specs=out_specs,
        scratch_shapes=[pltpu.VMEM((NH, HD, HD), F32)],
        compiler_params=_cparams(dimension_semantics=("arbitrary",)),
    )(*args)


SMALL_ROWS = 56
ROWS_MOD_X = (0, 1, 8, 16, 17, 18)
ROWS_MOD_C = (2, 3)
ROW_PRE1, ROW_POST1, ROW_ONORM, ROW_PRE2, ROW_POST2, ROW_LB, ROW_BGK, ROW_WGK = 4, 9, 10, 19, 20, 24, 32, 40
ROW_LOSS = 21


def _reduce_small(gathered, lb_full, name):
    _, _, d = gathered.shape

    def body(g_ref, lb_ref, sum_ref, dmod_ref, dbmod_ref, dlb_ref):
        total = g_ref[0]
        for b in range(1, N_DEV):
            total = total + g_ref[b]
        sum_ref[...] = total
        dmod_ref[...] = jnp.zeros_like(dmod_ref)
        for m in range(N_MOD):
            col = slice(m * d, (m + 1) * d)
            acc = jnp.zeros((1, d), F32)
            for b in range(N_DEV):
                row = g_ref[b, ROWS_MOD_X[m]:ROWS_MOD_X[m] + 1, :]
                dmod_ref[b:b + 1, col] = row
                acc = acc + row
            if m < 2:
                ctx_row = total[ROWS_MOD_C[m]:ROWS_MOD_C[m] + 1, :]
                dmod_ref[8:9, col] = ctx_row
                acc = acc + ctx_row
            dbmod_ref[:, col] = acc
        lbv = lb_ref[...]
        for dd in range(2):
            lb = _sigmoid(lbv[dd:dd + 1, :] - lbv[2 + dd:3 + dd, :])
            gl = total[ROW_LB:ROW_LB + 1, dd * HW:(dd + 1) * HW] * lb * (1.0 - lb)
            dlb_ref[dd:dd + 1, :] = gl
            dlb_ref[2 + dd:3 + dd, :] = -gl

    return pl.pallas_call(
        body, name=name,
        out_shape=[jax.ShapeDtypeStruct((SMALL_ROWS, d), F32), jax.ShapeDtypeStruct((16, N_MOD * d), F32),
                   jax.ShapeDtypeStruct((1, N_MOD * d), F32), jax.ShapeDtypeStruct((4, HW), F32)],
        in_specs=[VMEM_SPEC] * 2, out_specs=[VMEM_SPEC] * 4, compiler_params=_cparams(),
    )(gathered, lb_full)


def _c_ctx_grad(gathered, c_ctx_row, name):
    def body(g_ref, c_ref, o_ref):
        acc = g_ref[0, 0:1, :]
        for chip in range(1, N_CHIP):
            acc = acc + g_ref[2 * chip, 0:1, :]
        o_ref[...] = acc * _dsilu(c_ref[...])

    return pl.pallas_call(
        body, name=name, out_shape=jax.ShapeDtypeStruct(c_ctx_row.shape, F32),
        in_specs=[VMEM_SPEC] * 2, out_specs=VMEM_SPEC, compiler_params=_cparams(),
    )(gathered, c_ctx_row)


def _relayout_w_in(w):
    pad = jnp.zeros((w.shape[0], 128 - 2 * RANK), w.dtype)
    return jnp.concatenate([w[:, :9 * HW], w[:, 9 * HW + 2 * RANK:], w[:, 9 * HW:9 * HW + 2 * RANK], pad], axis=1)


def _relayout_w_in_rows(wt):
    pad = jnp.zeros((128 - 2 * RANK, wt.shape[1]), wt.dtype)
    return jnp.concatenate([wt[:9 * HW], wt[9 * HW + 2 * RANK:], wt[9 * HW:9 * HW + 2 * RANK], pad], axis=0)


def _w_in_grad_rows(g_main, g_lr):
    return jnp.concatenate([g_main[:9 * HW], g_lr[:2 * RANK], g_main[9 * HW:]], axis=0)


def _w_in_grad_blocks(g_main, g_lr, n_blocks):
    lr0 = 9 * HW
    n = (g_main.shape[1] + 2 * RANK) // n_blocks

    def cols(lo, hi):
        out = []
        if lo < lr0:
            out.append(g_main[:, lo:min(hi, lr0)])
        if hi > lr0 and lo < lr0 + 2 * RANK:
            out.append(g_lr[:, max(lo, lr0) - lr0:min(hi, lr0 + 2 * RANK) - lr0])
        if hi > lr0 + 2 * RANK:
            out.append(g_main[:, max(lo, lr0 + 2 * RANK) - 2 * RANK:hi - 2 * RANK])
        return out

    return jnp.stack([jnp.concatenate(cols(j * n, (j + 1) * n), axis=1) for j in range(n_blocks)])


def _blocked(full, n_blocks):
    k, n = full.shape
    return full.reshape(k, n_blocks, n // n_blocks).transpose(1, 0, 2)


def _unblocked(blocks):
    nb, k, n = blocks.shape
    return blocks.transpose(1, 0, 2).reshape(k, nb * n)


def _sample_front(x0, ctx0, modc, modx, norm_pre1, lb_full, gla_side, w_in_r):
    ctx_len = ctx0.shape[0]
    n_ctx_tiles = ctx_len // TM
    n_ctx_chunks = ctx_len // CHUNK
    h1, p = _in_projection(ctx0, x0, modc, modx, norm_pre1, w_in_r, n_ctx_tiles, "in_projection")
    o_hg_fw, st_hg_fw, o_hg_bw, st_hg_bw = _scan_fwd_both(p, lb_full, n_ctx_chunks, "hg", "scan_hg")
    o_gla_fw, st_gla_fw, o_gla_bw, st_gla_bw = _scan_fwd_both(p, gla_side, n_ctx_chunks, "gla", "scan_gla")
    return dict(h1=h1, p=p, o_list=[o_hg_fw, o_hg_bw, o_gla_fw, o_gla_bw],
                states=[st_hg_fw, st_hg_bw, st_gla_fw, st_gla_bw])


def _sample_back(reduce, front, x0, ctx0, target0, modc, modx, norm_pre1, norms, onorms, lb_full, gla_side, w_in_r,
                 wbh, wbg, wout, wg, wu, wd):
    seq, d = x0.shape
    ctx_len = ctx0.shape[0]
    n_ctx_tiles = ctx_len // TM
    n_tiles = (ctx_len + seq) // TM
    n_ctx_chunks = ctx_len // CHUNK
    h1, p, o_list = front["h1"], front["p"], front["o_list"]
    st_hg_fw, st_hg_bw, st_gla_fw, st_gla_bw = front["states"]
    z2, y1, merged, og_hg, og_gla = _mixer_tail_fwd(x0, p, o_list, modx, norms, onorms, wbh, wbg, wout, n_ctx_tiles,
                                                    "mixer_tail")
    loss_part, dz2, h2, a_act, du, dv, dy2, stat_ffn = _ffn_fwd_bwd(z2, modx, norms, wg, wu, wd, target0, "ffn")
    dff = wg.shape[0]
    tok = reduce("ffn", [_weight_grad(du, h2, "grad_w_ff_gate", tk=dff // 2, tn=d),
                         _weight_grad(dv, h2, "grad_w_ff_up", tk=dff // 2, tn=d),
                         _weight_grad(a_act, dy2, "grad_w_ff_down", tk=dff // 2)])

    (d_ohg, d_ogla, d_hgate, d_ggate, d_ghg, d_ggla, dy1, db_hg, db_gla, stat_mix) = _mixer_tail_bwd(
        x0, p, o_list, dz2, y1, modx + tok, norms, onorms, wbh, wbg, wout, n_ctx_tiles, n_tiles, "mixer_tail_bwd")
    tok = reduce("mix", [_weight_grad(og_hg, db_hg, "grad_w_br_hg"), _weight_grad(og_gla, db_gla, "grad_w_br_gla"),
                         _weight_grad(merged, dy1, "grad_w_out")])
    lb_b = lb_full + tok
    gla_b = [(wgk, bias + tok) for wgk, bias in gla_side]
    (dhq_f, dhi_f, dhf_f, dlb_f, dhq_b, dhi_b, dhf_b, dlb_b) = _scan_bwd_both(
        p, lb_b, (st_hg_fw, st_hg_bw), d_ohg, n_ctx_chunks, "hg", "scan_hg_bwd")
    (dgq_f, dgk_f, dgv_f, dlr_f, dwgk_f, dbgk_f, dgq_b, dgk_b, dgv_b, dlr_b, dwgk_b, dbgk_b) = _scan_bwd_both(
        p, gla_b, (st_gla_fw, st_gla_bw), d_ogla, n_ctx_chunks, "gla", "scan_gla_bwd")
    pieces = [dhq_f, dhq_b, dhi_f, dhi_b, dhf_f, dhf_b, d_hgate, dgq_f, dgq_b, dgk_f, dgk_b, dgv_f, dgv_b, d_ggate,
              d_ghg, d_ggla, dlr_f, dlr_b]
    dp, grad_x, stat_in = _in_projection_bwd(ctx0, x0, dz2, modc, modx, norm_pre1, w_in_r, pieces, n_ctx_tiles,
                                             "in_projection_bwd")

    reduce("small", dict(stat_in=stat_in, stat_mix=stat_mix, stat_ffn=stat_ffn, dlb=(dlb_f, dlb_b),
                         dwgk=(dwgk_f, dwgk_b), dbgk=(dbgk_f, dbgk_b)))
    reduce("in", [_w_in_grad(dp, h1, w_in_r.shape[0], "grad_w_in")])
    return dict(
        loss_part=loss_part, grad_x=grad_x, stat_in=stat_in, stat_mix=stat_mix, stat_ffn=stat_ffn,
        dlb=(dlb_f, dlb_b), dwgk=(dwgk_f, dwgk_b), dbgk=(dbgk_f, dbgk_b))


def kernel(x, c, ctx, c_ctx, w_mod, b_mod, norm_pre1, norm_post1, norm_pre2, norm_post2, w_in, hg_lb, hg_onorm, gla_w_gk, gla_b_gk, gla_onorm, w_br_hg, w_br_gla, w_out, w_ff_gate, w_ff_up, w_ff_down, loss_target, m_c_ctx, m_w_mod, m_b_mod, m_norm_pre1, m_norm_post1, m_norm_pre2, m_norm_post2, m_w_in, m_hg_lb, m_hg_onorm, m_gla_w_gk, m_gla_b_gk, m_gla_onorm, m_w_br_hg, m_w_br_gla, m_w_out, m_w_ff_gate, m_w_ff_up, m_w_ff_down, v_c_ctx, v_w_mod, v_b_mod, v_norm_pre1, v_norm_post1, v_norm_pre2, v_norm_post2, v_w_in, v_hg_lb, v_hg_onorm, v_gla_w_gk, v_gla_b_gk, v_gla_onorm, v_w_br_hg, v_w_br_gla, v_w_out, v_w_ff_gate, v_w_ff_up, v_w_ff_down):
    seq, d = x.shape[1], x.shape[2]
    ctx_len = ctx.shape[1]
    assert seq % TM == 0 and ctx_len % TM == 0 and d == 2 * HW
    ax, ay, ac = lax.axis_index("x"), lax.axis_index("y"), lax.axis_index("c")
    chip = 2 * ax + ay
    dev = 2 * chip + ac
    c_arr = jnp.reshape(ac, (1,)).astype(jnp.int32)
    chip_arr = jnp.reshape(chip, (1,)).astype(jnp.int32)
    transposed = ("w_in", "w_ff_gate", "w_ff_up")
    view = lambda a, nm: a[0].T if nm in transposed else a[0]

    sems_in, lands_in, token_in0 = _blocks_start([_cast_into_blocks(chip_arr, view(w_in, "w_in"), "cast_w_in")],
                                                 "gather_w_in_start")

    nc = d // 128
    pad8 = lambda a: jnp.pad(a, ((0, -a.shape[0] % 8), (0, 0)))
    small1 = jnp.concatenate([c.reshape(nc, 128) + token_in0[0, 0], pad8(hg_lb.reshape(4, 128)),
                              gla_w_gk.reshape(2 * RANK, 128), pad8(gla_b_gk.reshape(2, 128))], axis=0)
    got1 = _allgather8(small1, "gather_small_params")
    c_all = got1[:, :nc, :].reshape(N_DEV, d)
    per_chip = got1[0::2]
    lb_full = per_chip[:, nc:nc + 4, :].transpose(1, 0, 2).reshape(4, HW)
    wgk_full = per_chip[:, nc + 8:nc + 8 + 2 * RANK, :].transpose(1, 0, 2).reshape(2, RANK, HW)
    bgk_full = per_chip[:, nc + 8 + 2 * RANK:nc + 10 + 2 * RANK, :].transpose(1, 0, 2).reshape(2, HW)
    wgk_pad = [jnp.zeros((128, HW), F32).at[dd * RANK:(dd + 1) * RANK].set(wgk_full[dd]) for dd in range(2)]
    bgk = [bgk_full[dd:dd + 1] for dd in range(2)]

    n_mod_cols = w_mod.shape[2]
    cond = jnp.concatenate([c_all, pad8(c_ctx.reshape(1, d))], axis=0)
    b_cols = lax.dynamic_slice(b_mod, (0, chip * n_mod_cols), (1, n_mod_cols))
    mod_part = _mod_forward(cond, w_mod[0], b_cols, "mod_forward")
    mod_got = _allgather8(mod_part, "gather_mod")
    mod_all = mod_got[0::2].transpose(1, 0, 2).reshape(16, N_CHIP * n_mod_cols)
    modx = pad8(lax.dynamic_slice(mod_all, (dev, 0), (1, N_MOD * d)).reshape(N_MOD, d))
    modc = pad8(mod_all[8].reshape(N_MOD, d))

    blocks = [_cast_into_blocks(chip_arr, view(w_, nm), "cast_" + nm) for w_, nm in (
        (w_br_hg, "w_br_hg"), (w_br_gla, "w_br_gla"), (w_out, "w_out"), (w_ff_gate, "w_ff_gate"),
        (w_ff_up, "w_ff_up"), (w_ff_down, "w_ff_down"))]
    lands_in = _blocks_wait(sems_in, lands_in, [mod_got], "gather_w_in_wait")
    gathered_in = _blocks_finish(lands_in, "gather_w_in_finish")
    sems, lands, token = _blocks_start(blocks, "gather_rest_start", after=[gathered_in[0]])
    w_in_r = gathered_in[0].reshape(-1, d)

    norms = jnp.concatenate([norm_pre1, norm_post1, norm_pre2, norm_post2, jnp.zeros((4, d), F32)], axis=0)
    onorms = jnp.zeros((8, d), F32).at[0, :HD].set(hg_onorm[0]).at[1, :HD].set(gla_onorm[0])
    gla_side = [(wgk_pad[dd], bgk[dd]) for dd in range(2)]
    modx = modx + token[0, 0]
    front = _sample_front(x[0], ctx[0], modc, modx, norm_pre1, lb_full, gla_side, w_in_r)
    lands = _blocks_wait(sems, lands, front["o_list"], "gather_rest_wait")
    gathered = _blocks_finish(lands, "gather_rest_finish")
    wbh, wbg = _unblocked(gathered[0]), _unblocked(gathered[1])
    wout = gathered[2].reshape(d, d)
    wg, wu, wd = (gathered[i].reshape(-1, d) for i in (3, 4, 5))
    dff = wg.shape[0]
    groups = {"ffn": ["w_ff_gate", "w_ff_up", "w_ff_down"], "mix": ["w_br_hg", "w_br_gla", "w_out"], "in": ["w_in"]}
    row_sharded = {"w_out": d // N_CHIP, "w_ff_down": dff // N_CHIP, "w_ff_gate": dff // N_CHIP,
                   "w_ff_up": dff // N_CHIP, "w_in": w_in.shape[2]}
    in_flight = {}

    small = {}

    def reduce_small(stats):
        small2 = jnp.concatenate([
            stats["stat_in"], stats["stat_mix"], stats["stat_ffn"],
            jnp.concatenate(stats["dlb"], axis=1), jnp.concatenate(stats["dbgk"], axis=1),
            jnp.concatenate([stats["dwgk"][0][0:RANK], stats["dwgk"][1][RANK:2 * RANK]], axis=1)], axis=0)
        assert small2.shape[0] == SMALL_ROWS
        got2 = _allgather8(small2, "gather_small_grads")
        total, dmod_all, g_b_mod, g_lb_full = _reduce_small(got2, lb_full, "reduce_small")
        dmod_cols = lax.dynamic_slice(dmod_all, (0, chip * n_mod_cols), (16, n_mod_cols))
        g_w_mod, cctx_part = _mod_backward(cond, w_mod[0], dmod_cols, "mod_backward")
        got3 = _allgather8(cctx_part, "gather_c_ctx_grad")
        g_c_ctx = _c_ctx_grad(got3, c_ctx.reshape(1, d), "c_ctx_grad")
        small.update(total=total, g_b_mod=g_b_mod, g_lb_full=g_lb_full, g_w_mod=g_w_mod, g_c_ctx=g_c_ctx)

    def reduce(group, grads):
        if group == "small":
            return reduce_small(grads)
        nms = groups[group]
        full = [g.reshape(N_CHIP, row_sharded[nm], d) if nm in row_sharded else _blocked(g, N_CHIP)
                for g, nm in zip(grads, nms)]
        from_sibling = _send_other_half(full, "grads_to_sibling_" + group)
        pairs = [_pair_sum(c_arr, f, r_, "pair_sum_" + nm) for f, r_, nm in zip(full, from_sibling, nms)]
        after = [small["g_c_ctx"], small["total"]] if group == "in" else []
        sems_, pairs, lands_, token_ = _scatter_start(pairs, "grads_to_owner_start_" + group, after)
        in_flight[group] = (sems_, pairs, lands_, token_)
        return token_[0, 0]

    r = _sample_back(reduce, front, x[0], ctx[0], loss_target[0], modc, modx, norm_pre1, norms, onorms, lb_full,
                     gla_side, w_in_r, wbh, wbg, wout, wg, wu, wd)
    loss_part, grad_x, stat_in, stat_mix, stat_ffn = (r[k] for k in ("loss_part", "grad_x", "stat_in", "stat_mix",
                                                                     "stat_ffn"))
    (dlb_f, dlb_b), (dwgk_f, dwgk_b), (dbgk_f, dbgk_b) = r["dlb"], r["dwgk"], r["dbgk"]

    weights = dict(w_in=(w_in, m_w_in, v_w_in), w_br_hg=(w_br_hg, m_w_br_hg, v_w_br_hg),
                   w_br_gla=(w_br_gla, m_w_br_gla, v_w_br_gla), w_out=(w_out, m_w_out, v_w_out),
                   w_ff_gate=(w_ff_gate, m_w_ff_gate, v_w_ff_gate), w_ff_up=(w_ff_up, m_w_ff_up, v_w_ff_up),
                   w_ff_down=(w_ff_down, m_w_ff_down, v_w_ff_down))
    names = ["w_in", "w_br_hg", "w_br_gla", "w_out", "w_ff_gate", "w_ff_up", "w_ff_down"]
    big = {}

    def finish(group, after):
        sems_, pairs, lands_, _ = in_flight[group]
        pairs, lands_ = _scatter_wait(sems_, pairs, lands_, after, "grads_to_owner_wait_" + group)
        own_half = [_sum_owner(chip_arr, pr, g, "chip_sum_" + nm) for pr, g, nm in zip(pairs, lands_, groups[group])]
        other_half = _swap_with_sibling(own_half, "halves_to_sibling_" + group)
        for nm, own, oth in zip(groups[group], own_half, other_half):
            w_, m_, v_ = (view(a, nm) for a in weights[nm])
            res = _adamw_halves(c_arr, own, oth, w_, m_, v_, "adamw_" + nm)
            big[nm] = [r_.T[None] if nm in transposed else r_[None] for r_ in res]
        return big[groups[group][-1]][1]

    token_in = in_flight["in"][3]
    done_ffn = finish("ffn", [token_in])
    done_mix = finish("mix", [done_ffn])

    total, g_b_mod, g_lb_full, g_w_mod, g_c_ctx = (small[k] for k in ("total", "g_b_mod", "g_lb_full", "g_w_mod",
                                                                      "g_c_ctx"))
    g_pre1, g_post1, g_pre2, g_post2 = (total[r_:r_ + 1] for r_ in (ROW_PRE1, ROW_POST1, ROW_PRE2, ROW_POST2))
    g_hg_on, g_gla_on = total[ROW_ONORM:ROW_ONORM + 1, 0:HD], total[ROW_ONORM:ROW_ONORM + 1, HD:2 * HD]
    n_lb = hg_lb.shape[2]
    g_hg_lb = lax.dynamic_slice(g_lb_full, (0, chip * n_lb), (4, n_lb))
    g_bgk = lax.dynamic_slice(total[ROW_BGK:ROW_BGK + 1].reshape(2, HW), (0, chip * n_lb), (2, n_lb))
    g_wgk_full = total[ROW_WGK:ROW_WGK + RANK].reshape(RANK, 2, HW).transpose(1, 0, 2).reshape(2 * RANK, HW)
    g_wgk = lax.dynamic_slice(g_wgk_full, (0, chip * n_lb), (2 * RANK, n_lb))

    small_items = [
        (g_c_ctx, c_ctx.reshape(1, d), m_c_ctx.reshape(1, d), v_c_ctx.reshape(1, d)),
        (g_b_mod, b_mod, m_b_mod, v_b_mod),
        (g_pre1, norm_pre1, m_norm_pre1, v_norm_pre1),
        (g_post1, norm_post1, m_norm_post1, v_norm_post1),
        (g_pre2, norm_pre2, m_norm_pre2, v_norm_pre2),
        (g_post2, norm_post2, m_norm_post2, v_norm_post2),
        (g_hg_lb, hg_lb.reshape(4, n_lb), m_hg_lb.reshape(4, n_lb), v_hg_lb.reshape(4, n_lb)),
        (g_hg_on, hg_onorm, m_hg_onorm, v_hg_onorm),
        (g_wgk, gla_w_gk.reshape(2 * RANK, n_lb), m_gla_w_gk.reshape(2 * RANK, n_lb), v_gla_w_gk.reshape(2 * RANK, n_lb)),
        (g_bgk, gla_b_gk.reshape(2, n_lb), m_gla_b_gk.reshape(2, n_lb), v_gla_b_gk.reshape(2, n_lb)),
        (g_gla_on, gla_onorm, m_gla_onorm, v_gla_onorm),
    ]
    small_res = _adamw_whole(small_items, "adamw_small")
    mod_res = _adamw_tiled(g_w_mod, w_mod[0], m_w_mod[0], v_w_mod[0], "adamw_w_mod")
    finish("in", [done_mix, mod_res[0], small_res[0][0]])

    loss = total[ROW_LOSS, 0]

    shapes = dict(c_ctx=c_ctx.shape, b_mod=b_mod.shape, norm_pre1=norm_pre1.shape, norm_post1=norm_post1.shape,
                  norm_pre2=norm_pre2.shape, norm_post2=norm_post2.shape, hg_lb=hg_lb.shape, hg_onorm=hg_onorm.shape,
                  gla_w_gk=gla_w_gk.shape, gla_b_gk=gla_b_gk.shape, gla_onorm=gla_onorm.shape)
    small_names = ["c_ctx", "b_mod", "norm_pre1", "norm_post1", "norm_pre2", "norm_post2", "hg_lb", "hg_onorm",
                   "gla_w_gk", "gla_b_gk", "gla_onorm"]
    grads, deltas, new_m, new_v = {}, {}, {}, {}
    for nm, item, res in zip(small_names, small_items, small_res):
        grads[nm] = item[0].reshape(shapes[nm])
        deltas[nm], new_m[nm], new_v[nm] = (r.reshape(shapes[nm]) for r in res)
    grads["w_mod"] = g_w_mod[None]
    deltas["w_mod"], new_m["w_mod"], new_v["w_mod"] = (r[None] for r in mod_res)
    for nm in names:
        grads[nm], deltas[nm], new_m[nm], new_v[nm] = big[nm]
    order = ["c_ctx", "w_mod", "b_mod", "norm_pre1", "norm_post1", "norm_pre2", "norm_post2", "w_in", "hg_lb",
             "hg_onorm", "gla_w_gk", "gla_b_gk", "gla_onorm", "w_br_hg", "w_br_gla", "w_out", "w_ff_gate", "w_ff_up",
             "w_ff_down"]
    return (loss, grad_x[None], *[grads[n] for n in order], *[deltas[n] for n in order],
            *[new_m[n] for n in order], *[new_v[n] for n in order])


def _weight_grad_cols(xs, dy, n_cols, name, tn=512):
    rows = dy.shape[0]
    k = tk = xs.shape[1]

    return pl.pallas_call(
        functools.partial(_transposed_lhs_matmul), name=name, grid=(k // tk, n_cols // tn),
        out_shape=jax.ShapeDtypeStruct((k, n_cols), F32),
        in_specs=[pl.BlockSpec((rows, tk), lambda i, j: (0, i)), pl.BlockSpec((rows, tn), lambda i, j: (0, j))],
        out_specs=pl.BlockSpec((tk, tn), lambda i, j: (i, j)),
        scratch_shapes=[pltpu.VMEM((tk, rows), BF16)],
        compiler_params=_cparams(dimension_semantics=("parallel", "arbitrary")),
    )(xs, dy)
```

```python
import functools

import jax
import jax.numpy as jnp
from jax import lax
from jax.experimental import pallas as pl
from jax.experimental.pallas import tpu as pltpu

F32 = jnp.float32
BF16 = jnp.bfloat16
HIGHEST = lax.Precision.HIGHEST
MESH = pl.DeviceIdType.MESH

EPS = 1e-6
CHUNK = 64
SUB = 16
NSUB = CHUNK // SUB
NH = 4
HD = 128
HW = NH * HD
RANK = 16
GATE_NORM = 16.0
N_MOD = 6
TM = 256
TM_FFN = 128
N_DEV = 8
N_CHIP = 4
VMEM_LIMIT = 56 * 1024 * 1024

ADAM_LR = 0.001
ADAM_B1 = 0.9
ADAM_B2 = 0.999
ADAM_EPS = 1e-08
ADAM_WD = 0.01
ADAM_STEP = 10

VMEM_SPEC = pl.BlockSpec(memory_space=pltpu.VMEM)
ANY_SPEC = pl.BlockSpec(memory_space=pl.ANY)
HBM_SPEC = pl.BlockSpec(memory_space=pltpu.HBM)
SEM_SPEC = pl.BlockSpec(memory_space=pltpu.SEMAPHORE)
EFFECT = pltpu.SideEffectType.DATAFLOW_SIDE_EFFECTING


def _cparams(**kw):
    return pltpu.CompilerParams(vmem_limit_bytes=VMEM_LIMIT, **kw)


def _dot(a, b):
    return jnp.dot(a.astype(BF16), b.astype(BF16), preferred_element_type=F32)


def _dot_nt(a, b):
    return lax.dot_general(a.astype(BF16), b.astype(BF16), (((1,), (1,)), ((), ())), preferred_element_type=F32)


def _dot_tn(a, b):
    return lax.dot_general(a.astype(BF16), b.astype(BF16), (((0,), (0,)), ((), ())), preferred_element_type=F32)


def _sigmoid(x):
    return 1.0 / (1.0 + jnp.exp(-x))


def _silu(x):
    return x * _sigmoid(x)


def _dsilu(x):
    s = _sigmoid(x)
    return s * (1.0 + x * (1.0 - s))


def _log_sigmoid(x):
    return jnp.minimum(x, 0.0) - jnp.log(1.0 + jnp.exp(-jnp.abs(x)))


def _colsum(a):
    return jnp.sum(a, axis=0, keepdims=True)


def _rms(a):
    r = lax.rsqrt(jnp.mean(a * a, axis=-1, keepdims=True) + EPS)
    return a * r, r


def _rms_bwd(dn, n, r):
    return r * (dn - n * jnp.mean(dn * n, axis=-1, keepdims=True))


def _place():
    x, y, c = lax.axis_index("x"), lax.axis_index("y"), lax.axis_index("c")
    chips = [(1 - x, y), (x, 1 - y), (1 - x, 1 - y)]
    return x, y, c, chips


def _allgather8(v, name):
    rows, cols = v.shape

    def body(x_ref, out_ref, send_sems, recv_sems, local_sem):
        x, y, c, chips = _place()
        me, sibling = (x, y, c), (x, y, 1 - c)

        def blk(px, py, pc):
            return out_ref.at[4 * px + 2 * py + pc]

        def copy(k, block, to, src=None):
            return pltpu.make_async_remote_copy(
                src_ref=blk(*block) if src is None else src, dst_ref=blk(*block),
                send_sem=send_sems.at[k], recv_sem=recv_sems.at[k], device_id=to, device_id_type=MESH)

        mine = pltpu.make_async_copy(x_ref, blk(*me), local_sem)
        mine.start()
        first = [copy(0, me, sibling, src=x_ref)]
        first += [copy(1 + j, me, (*chip, c), src=x_ref) for j, chip in enumerate(chips)]
        for cp in first:
            cp.start()
        passed = [copy(4 + j, (*chip, c), sibling) for j, chip in enumerate(chips)]
        for j, chip in enumerate(chips):
            copy(1 + j, (*chip, c), me).wait_recv()
            passed[j].start()
        copy(0, sibling, me).wait_recv()
        for j, chip in enumerate(chips):
            copy(4 + j, (*chip, 1 - c), me).wait_recv()
        for cp in first + passed:
            cp.wait_send()
        mine.wait()

    return pl.pallas_call(
        body, name=name,
        out_shape=jax.ShapeDtypeStruct((N_DEV, rows, cols), v.dtype),
        in_specs=[VMEM_SPEC], out_specs=VMEM_SPEC,
        scratch_shapes=[pltpu.SemaphoreType.DMA((7,)), pltpu.SemaphoreType.DMA((7,)), pltpu.SemaphoreType.DMA],
    )(v)


def _cast_into_blocks(chip_arr, w, name):
    rows, cols = w.shape
    tr = _row_tile(rows, 16, 256)

    def body(chip_ref, w_ref, o_ref):
        o_ref[0] = w_ref[...].astype(BF16)

    return pl.pallas_call(
        body, name=name,
        grid_spec=pltpu.PrefetchScalarGridSpec(
            num_scalar_prefetch=1, grid=(rows // tr,),
            in_specs=[pl.BlockSpec((tr, cols), lambda i, chip_ref: (i, 0))],
            out_specs=pl.BlockSpec((1, tr, cols), lambda i, chip_ref: (chip_ref[0], i, 0))),
        out_shape=jax.ShapeDtypeStruct((N_CHIP, rows, cols), BF16),
        compiler_params=_cparams(dimension_semantics=("parallel",)),
    )(chip_arr, w)


def _halved_by_rows(shape):
    return (shape[1] // 2) % 16 == 0


def _half_of(ref, pc, block=None):
    lead = slice(None) if block is None else block
    if _halved_by_rows(ref.shape):
        h = ref.shape[1] // 2
        return ref.at[lead, pl.ds(pl.multiple_of(pc * h, 16), h), :]
    h = ref.shape[2] // 2
    return ref.at[lead, :, pl.ds(pl.multiple_of(pc * h, 128), h)]


def _half_shape(shape):
    return (shape[0], shape[1] // 2, shape[2]) if _halved_by_rows(shape) else (shape[0], shape[1], shape[2] // 2)


def _half_rows(ref, chip_id, pc):
    return _half_of(ref, pc, chip_id)


def _gather_blocks(lands, name, after=()):
    n = len(lands)
    n_in = n + len(after)

    def body(*refs):
        outs = refs[n_in:n_in + n]
        send_sems, recv_sems = refs[n_in + n:]
        x, y, c, chips = _place()
        me_chip = 2 * x + y
        sibling = (x, y, 1 - c)

        def copy(k, j, chip_id, pc, to):
            return pltpu.make_async_remote_copy(
                src_ref=_half_rows(outs[k], chip_id, pc), dst_ref=_half_rows(outs[k], chip_id, pc),
                send_sem=send_sems.at[k, j], recv_sem=recv_sems.at[k, j], device_id=to, device_id_type=MESH)

        started = []
        for k in range(n):
            for j, chip in enumerate(chips):
                cp = copy(k, j, me_chip, c, (*chip, c))
                cp.start()
                started.append(cp)
        for k in range(n):
            for j, (px, py) in enumerate(chips):
                copy(k, j, 2 * px + py, c, sibling).wait_recv()
                cp = copy(k, 3 + j, 2 * px + py, c, sibling)
                cp.start()
                started.append(cp)
        for k in range(n):
            for j, (px, py) in enumerate(chips):
                copy(k, 3 + j, 2 * px + py, 1 - c, sibling).wait_recv()
        for cp in started:
            cp.wait_send()

    return pl.pallas_call(
        body, name=name,
        out_shape=[jax.ShapeDtypeStruct(l.shape, l.dtype) for l in lands],
        in_specs=[ANY_SPEC] * n_in, out_specs=[ANY_SPEC] * n,
        input_output_aliases={i: i for i in range(n)},
        scratch_shapes=[pltpu.SemaphoreType.DMA((n, 6)), pltpu.SemaphoreType.DMA((n, 6))],
    )(*lands, *after)


def _hbm(a):
    return pltpu.with_memory_space_constraint(a, pltpu.HBM)


def _blocks_start(lands, name, after=()):
    n = len(lands)
    n_sem = 3 * n
    first = n + len(after)

    def body(*refs):
        lnd = refs[:n]
        send_sems, recv_sems = refs[first:first + n_sem], refs[first + n_sem:first + 2 * n_sem]
        token = refs[-1]
        x, y, c, chips = _place()
        me_chip = 2 * x + y
        for k in range(n):
            for j, chip in enumerate(chips):
                pltpu.make_async_remote_copy(
                    src_ref=_half_rows(lnd[k], me_chip, c), dst_ref=_half_rows(lnd[k], me_chip, c),
                    send_sem=send_sems[3 * k + j], recv_sem=recv_sems[3 * k + j],
                    device_id=(*chip, c), device_id_type=MESH).start()
        token[...] = jnp.zeros_like(token)

    out = pl.pallas_call(
        body, name=name,
        out_shape=(*[pltpu.SemaphoreType.DMA(())] * (2 * n_sem),
                   *[pltpu.HBM(l.shape, l.dtype) for l in lands],
                   jax.ShapeDtypeStruct((8, 128), F32)),
        in_specs=[HBM_SPEC] * n + [ANY_SPEC] * len(after),
        out_specs=(*[SEM_SPEC] * (2 * n_sem), *[HBM_SPEC] * n, VMEM_SPEC),
        input_output_aliases={i: 2 * n_sem + i for i in range(n)},
        compiler_params=pltpu.CompilerParams(has_side_effects=EFFECT),
    )(*[_hbm(l) for l in lands], *after)
    return list(out[:2 * n_sem]), list(out[2 * n_sem:2 * n_sem + n]), out[-1]


def _blocks_wait(sems, lands, after, name):
    n = len(lands)
    n_sem = 3 * n

    def body(*refs):
        lnd = refs[:n]
        s_sems, r_sems = refs[n:n + n_sem], refs[n + n_sem:n + 2 * n_sem]
        x, y, c, chips = _place()
        me_chip = 2 * x + y
        for k in range(n):
            for j, (px, py) in enumerate(chips):
                cp = pltpu.make_async_remote_copy(
                    src_ref=_half_rows(lnd[k], me_chip, c), dst_ref=_half_rows(lnd[k], 2 * px + py, c),
                    send_sem=s_sems[3 * k + j], recv_sem=r_sems[3 * k + j],
                    device_id=(px, py, c), device_id_type=MESH)
                cp.wait_send()
                cp.wait_recv()

    out = pl.pallas_call(
        body, name=name,
        out_shape=tuple(pltpu.HBM(l.shape, l.dtype) for l in lands),
        in_specs=[HBM_SPEC] * n + [SEM_SPEC] * (2 * n_sem) + [ANY_SPEC] * len(after),
        out_specs=[HBM_SPEC] * n,
        input_output_aliases={i: i for i in range(n)},
        compiler_params=pltpu.CompilerParams(has_side_effects=EFFECT),
    )(*lands, *sems, *after)
    return list(out)


def _blocks_finish(lands, name):
    n = len(lands)

    def body(*refs):
        lnd = refs[n:2 * n]
        send_sems, recv_sems = refs[2 * n:]
        x, y, c, chips = _place()
        sibling = (x, y, 1 - c)

        def copy(k, j, chip_id, pc):
            return pltpu.make_async_remote_copy(
                src_ref=_half_rows(lnd[k], chip_id, pc), dst_ref=_half_rows(lnd[k], chip_id, pc),
                send_sem=send_sems.at[k, j], recv_sem=recv_sems.at[k, j], device_id=sibling, device_id_type=MESH)

        started = []
        for k in range(n):
            for j, (px, py) in enumerate(chips):
                cp = copy(k, j, 2 * px + py, c)
                cp.start()
                started.append(cp)
        for k in range(n):
            for j, (px, py) in enumerate(chips):
                copy(k, j, 2 * px + py, 1 - c).wait_recv()
        for cp in started:
            cp.wait_send()

    out = pl.pallas_call(
        body, name=name,
        out_shape=[jax.ShapeDtypeStruct(l.shape, l.dtype) for l in lands],
        in_specs=[ANY_SPEC] * n, out_specs=[ANY_SPEC] * n,
        input_output_aliases={i: i for i in range(n)},
        scratch_shapes=[pltpu.SemaphoreType.DMA((n, 3)), pltpu.SemaphoreType.DMA((n, 3))],
    )(*lands)
    return list(out)


def _gather_start(shards, name):
    n = len(shards)
    n_sem = 3 * n

    def body(*refs):
        ins, lands = refs[:n], refs[n:2 * n]
        send_sems, recv_sems = refs[2 * n:2 * n + n_sem], refs[2 * n + n_sem:2 * n + 2 * n_sem]
        token = refs[-1]
        x, y, c, chips = _place()
        me_chip = 2 * x + y
        for k in range(n):
            h = shards[k].shape[0] // 2
            rows = pl.ds(pl.multiple_of(c * h, 8), h)
            for j, chip in enumerate(chips):
                pltpu.make_async_remote_copy(
                    src_ref=ins[k].at[rows, :], dst_ref=lands[k].at[me_chip, rows, :],
                    send_sem=send_sems[3 * k + j], recv_sem=recv_sems[3 * k + j],
                    device_id=(*chip, c), device_id_type=MESH).start()
        token[...] = jnp.zeros_like(token)

    lands = [_hbm(lax.empty((N_CHIP,) + s.shape, s.dtype)) for s in shards]
    out = pl.pallas_call(
        body, name=name,
        out_shape=(*[pltpu.SemaphoreType.DMA(())] * (2 * n_sem),
                   *[pltpu.HBM(s.shape, s.dtype) for s in shards],
                   *[pltpu.HBM(l.shape, l.dtype) for l in lands],
                   jax.ShapeDtypeStruct((8, 128), F32)),
        in_specs=[HBM_SPEC] * (2 * n),
        out_specs=(*[SEM_SPEC] * (2 * n_sem), *[HBM_SPEC] * (2 * n), VMEM_SPEC),
        input_output_aliases={i: 2 * n_sem + i for i in range(2 * n)},
        compiler_params=pltpu.CompilerParams(has_side_effects=EFFECT),
    )(*[_hbm(s) for s in shards], *lands)
    sems = list(out[:2 * n_sem])
    return sems, list(out[2 * n_sem:2 * n_sem + n]), list(out[2 * n_sem + n:2 * n_sem + 2 * n]), out[-1]


def _gather_wait(sems, shards, lands, after, name):
    n = len(shards)
    n_sem = 3 * n

    def body(*refs):
        ins, lnd = refs[:n], refs[n:2 * n]
        s_sems, r_sems = refs[2 * n:2 * n + n_sem], refs[2 * n + n_sem:2 * n + 2 * n_sem]
        x, y, c, chips = _place()
        for k in range(n):
            h = shards[k].shape[0] // 2
            rows = pl.ds(pl.multiple_of(c * h, 8), h)
            for j, (px, py) in enumerate(chips):
                cp = pltpu.make_async_remote_copy(
                    src_ref=ins[k].at[rows, :], dst_ref=lnd[k].at[2 * px + py, rows, :],
                    send_sem=s_sems[3 * k + j], recv_sem=r_sems[3 * k + j],
                    device_id=(px, py, c), device_id_type=MESH)
                cp.wait_send()
                cp.wait_recv()

    out = pl.pallas_call(
        body, name=name,
        out_shape=(*[pltpu.HBM(s.shape, s.dtype) for s in shards], *[pltpu.HBM(l.shape, l.dtype) for l in lands]),
        in_specs=[HBM_SPEC] * (2 * n) + [SEM_SPEC] * (2 * n_sem) + [ANY_SPEC],
        out_specs=[HBM_SPEC] * (2 * n),
        input_output_aliases={i: i for i in range(2 * n)},
        compiler_params=pltpu.CompilerParams(has_side_effects=EFFECT),
    )(*shards, *lands, *sems, after)
    return list(out[:n]), list(out[n:])


def _gather_finish(shards, lands, name):
    n = len(shards)

    def body(*refs):
        ins, lnd = refs[:n], refs[2 * n:3 * n]
        send_sems, recv_sems, local_sems = refs[3 * n:]
        x, y, c, chips = _place()
        me_chip = 2 * x + y
        sibling = (x, y, 1 - c)

        def half(k, chip_id, pc):
            h = shards[k].shape[0] // 2
            return lnd[k].at[chip_id, pl.ds(pl.multiple_of(pc * h, 8), h), :]

        def copy(k, j, chip_id, pc):
            return pltpu.make_async_remote_copy(
                src_ref=half(k, chip_id, pc), dst_ref=half(k, chip_id, pc),
                send_sem=send_sems.at[k, j], recv_sem=recv_sems.at[k, j], device_id=sibling, device_id_type=MESH)

        locals_, started = [], []
        for k in range(n):
            cp = pltpu.make_async_copy(ins[k], lnd[k].at[me_chip], local_sems.at[k])
            cp.start()
            locals_.append(cp)
            for j, (px, py) in enumerate(chips):
                cp = copy(k, j, 2 * px + py, c)
                cp.start()
                started.append(cp)
        for k in range(n):
            for j, (px, py) in enumerate(chips):
                copy(k, j, 2 * px + py, 1 - c).wait_recv()
        for cp in started:
            cp.wait_send()
        for cp in locals_:
            cp.wait()

    out = pl.pallas_call(
        body, name=name,
        out_shape=[jax.ShapeDtypeStruct(l.shape, l.dtype) for l in lands],
        in_specs=[ANY_SPEC] * (2 * n), out_specs=[ANY_SPEC] * n,
        input_output_aliases={n + i: i for i in range(n)},
        scratch_shapes=[pltpu.SemaphoreType.DMA((n, 3)), pltpu.SemaphoreType.DMA((n, 3)),
                        pltpu.SemaphoreType.DMA((n,))],
    )(*shards, *lands)
    return list(out)


def _send_other_half(arrs, name):
    n = len(arrs)

    def body(*refs):
        ins, outs = refs[:n], refs[n:2 * n]
        send_sems, recv_sems = refs[2 * n:]
        x, y, c, _ = _place()
        cps = []
        for k in range(n):
            cp = pltpu.make_async_remote_copy(
                src_ref=_half_of(ins[k], 1 - c), dst_ref=outs[k],
                send_sem=send_sems.at[k], recv_sem=recv_sems.at[k], device_id=(x, y, 1 - c), device_id_type=MESH)
            cp.start()
            cps.append(cp)
        for cp in cps:
            cp.wait()

    return pl.pallas_call(
        body, name=name,
        out_shape=[jax.ShapeDtypeStruct(_half_shape(a.shape), a.dtype) for a in arrs],
        in_specs=[ANY_SPEC] * n, out_specs=[ANY_SPEC] * n,
        scratch_shapes=[pltpu.SemaphoreType.DMA((n,)), pltpu.SemaphoreType.DMA((n,))],
    )(*arrs)


def _send_half_start(arrs, name):
    n = len(arrs)

    def body(*refs):
        ins, lnd = refs[:n], refs[n:2 * n]
        send_sems, recv_sems = refs[2 * n:3 * n], refs[3 * n:4 * n]
        token = refs[-1]
        x, y, c, _ = _place()
        for k in range(n):
            pltpu.make_async_remote_copy(
                src_ref=_half_of(ins[k], 1 - c), dst_ref=lnd[k], send_sem=send_sems[k], recv_sem=recv_sems[k],
                device_id=(x, y, 1 - c), device_id_type=MESH).start()
        token[...] = jnp.zeros_like(token)

    lands = [_hbm(lax.empty(_half_shape(a.shape), a.dtype)) for a in arrs]
    out = pl.pallas_call(
        body, name=name,
        out_shape=(*[pltpu.SemaphoreType.DMA(())] * (2 * n), *[pltpu.HBM(a.shape, a.dtype) for a in arrs],
                   *[pltpu.HBM(l.shape, l.dtype) for l in lands], jax.ShapeDtypeStruct((8, 128), F32)),
        in_specs=[HBM_SPEC] * (2 * n),
        out_specs=(*[SEM_SPEC] * (2 * n), *[HBM_SPEC] * (2 * n), VMEM_SPEC),
        input_output_aliases={i: 2 * n + i for i in range(2 * n)},
        compiler_params=pltpu.CompilerParams(has_side_effects=EFFECT),
    )(*[_hbm(a) for a in arrs], *lands)
    return list(out[:2 * n]), list(out[2 * n:3 * n]), list(out[3 * n:4 * n]), out[-1]


def _send_half_wait(sems, arrs, lands, after, name):
    n = len(arrs)

    def body(*refs):
        ins, lnd = refs[:n], refs[n:2 * n]
        s_sems, r_sems = refs[2 * n:3 * n], refs[3 * n:4 * n]
        x, y, c, _ = _place()
        for k in range(n):
            cp = pltpu.make_async_remote_copy(
                src_ref=_half_of(ins[k], 1 - c), dst_ref=lnd[k], send_sem=s_sems[k], recv_sem=r_sems[k],
                device_id=(x, y, 1 - c), device_id_type=MESH)
            cp.wait_send()
            cp.wait_recv()

    out = pl.pallas_call(
        body, name=name,
        out_shape=tuple(pltpu.HBM(a.shape, a.dtype) for a in list(arrs) + list(lands)),
        in_specs=[HBM_SPEC] * (2 * n) + [SEM_SPEC] * (2 * n) + [ANY_SPEC] * len(after),
        out_specs=[HBM_SPEC] * (2 * n),
        input_output_aliases={i: i for i in range(2 * n)},
        compiler_params=pltpu.CompilerParams(has_side_effects=EFFECT),
    )(*arrs, *lands, *sems, *after)
    return list(out[:n]), list(out[n:])


def _blocks_to_owner(arrs, name):
    n = len(arrs)

    def body(*refs):
        ins, outs = refs[:n], refs[n:2 * n]
        send_sems, recv_sems, local_sems = refs[2 * n:]
        x, y, c, chips = _place()
        me_chip = 2 * x + y
        locals_, started = [], []
        for k in range(n):
            cp = pltpu.make_async_copy(ins[k].at[me_chip], outs[k].at[me_chip], local_sems.at[k])
            cp.start()
            locals_.append(cp)

        def copy(k, j, src_block, dst_slot, to):
            return pltpu.make_async_remote_copy(
                src_ref=ins[k].at[src_block], dst_ref=outs[k].at[dst_slot],
                send_sem=send_sems.at[k, j], recv_sem=recv_sems.at[k, j], device_id=to, device_id_type=MESH)

        for k in range(n):
            for j, (px, py) in enumerate(chips):
                cp = copy(k, j, 2 * px + py, me_chip, (px, py, c))
                cp.start()
                started.append(cp)
        for k in range(n):
            for j, (px, py) in enumerate(chips):
                copy(k, j, me_chip, 2 * px + py, (px, py, c)).wait_recv()
        for cp in started:
            cp.wait_send()
        for cp in locals_:
            cp.wait()

    return pl.pallas_call(
        body, name=name,
        out_shape=[jax.ShapeDtypeStruct(a.shape, a.dtype) for a in arrs],
        in_specs=[ANY_SPEC] * n, out_specs=[ANY_SPEC] * n,
        scratch_shapes=[pltpu.SemaphoreType.DMA((n, 3)), pltpu.SemaphoreType.DMA((n, 3)),
                        pltpu.SemaphoreType.DMA((n,))],
    )(*arrs)


def _scatter_blocks(arrs, name):
    n = len(arrs)

    def body(*refs):
        ins, outs = refs[:n], refs[n:2 * n]
        send_sems, recv_sems = refs[2 * n:]
        x, y, c, chips = _place()
        me_chip = 2 * x + y

        def copy(k, j, src_block, dst_slot, to):
            return pltpu.make_async_remote_copy(
                src_ref=ins[k].at[src_block], dst_ref=outs[k].at[dst_slot],
                send_sem=send_sems.at[k, j], recv_sem=recv_sems.at[k, j], device_id=to, device_id_type=MESH)

        started = []
        for k in range(n):
            for j, (px, py) in enumerate(chips):
                cp = copy(k, j, 2 * px + py, me_chip, (px, py, c))
                cp.start()
                started.append(cp)
        for k in range(n):
            for j, (px, py) in enumerate(chips):
                copy(k, j, me_chip, 2 * px + py, (px, py, c)).wait_recv()
        for cp in started:
            cp.wait_send()

    return pl.pallas_call(
        body, name=name,
        out_shape=[jax.ShapeDtypeStruct(a.shape, a.dtype) for a in arrs],
        in_specs=[ANY_SPEC] * n, out_specs=[ANY_SPEC] * n,
        scratch_shapes=[pltpu.SemaphoreType.DMA((n, 3)), pltpu.SemaphoreType.DMA((n, 3))],
    )(*arrs)


def _scatter_start(arrs, name, after=()):
    n = len(arrs)
    n_sem = 3 * n
    first = 2 * n + len(after)

    def body(*refs):
        ins, lnd = refs[:n], refs[n:2 * n]
        send_sems, recv_sems = refs[first:first + n_sem], refs[first + n_sem:first + 2 * n_sem]
        token = refs[-1]
        x, y, c, chips = _place()
        me_chip = 2 * x + y
        for k in range(n):
            for j, (px, py) in enumerate(chips):
                pltpu.make_async_remote_copy(
                    src_ref=ins[k].at[2 * px + py], dst_ref=lnd[k].at[me_chip],
                    send_sem=send_sems[3 * k + j], recv_sem=recv_sems[3 * k + j],
                    device_id=(px, py, c), device_id_type=MESH).start()
        token[...] = jnp.zeros_like(token)

    lands = [_hbm(lax.empty(a.shape, a.dtype)) for a in arrs]
    out = pl.pallas_call(
        body, name=name,
        out_shape=(*[pltpu.SemaphoreType.DMA(())] * (2 * n_sem),
                   *[pltpu.HBM(a.shape, a.dtype) for a in arrs], *[pltpu.HBM(a.shape, a.dtype) for a in arrs],
                   jax.ShapeDtypeStruct((8, 128), F32)),
        in_specs=[HBM_SPEC] * (2 * n) + [ANY_SPEC] * len(after),
        out_specs=(*[SEM_SPEC] * (2 * n_sem), *[HBM_SPEC] * (2 * n), VMEM_SPEC),
        input_output_aliases={i: 2 * n_sem + i for i in range(2 * n)},
        compiler_params=pltpu.CompilerParams(has_side_effects=EFFECT),
    )(*[_hbm(a) for a in arrs], *lands, *after)
    base = 2 * n_sem
    return list(out[:base]), list(out[base:base + n]), list(out[base + n:base + 2 * n]), out[-1]


def _scatter_wait(sems, arrs, lands, after, name):
    n = len(arrs)
    n_sem = 3 * n

    def body(*refs):
        ins, lnd = refs[:n], refs[n:2 * n]
        s_sems, r_sems = refs[2 * n:2 * n + n_sem], refs[2 * n + n_sem:2 * n + 2 * n_sem]
        x, y, c, chips = _place()
        for k in range(n):
            for j, (px, py) in enumerate(chips):
                cp = pltpu.make_async_remote_copy(
                    src_ref=ins[k].at[2 * px + py], dst_ref=lnd[k].at[2 * px + py],
                    send_sem=s_sems[3 * k + j], recv_sem=r_sems[3 * k + j],
                    device_id=(px, py, c), device_id_type=MESH)
                cp.wait_send()
                cp.wait_recv()

    out = pl.pallas_call(
        body, name=name,
        out_shape=tuple(pltpu.HBM(a.shape, a.dtype) for a in list(arrs) + list(lands)),
        in_specs=[HBM_SPEC] * (2 * n) + [SEM_SPEC] * (2 * n_sem) + [ANY_SPEC] * len(after),
        out_specs=[HBM_SPEC] * (2 * n),
        input_output_aliases={i: i for i in range(2 * n)},
        compiler_params=pltpu.CompilerParams(has_side_effects=EFFECT),
    )(*arrs, *lands, *sems, *after)
    return list(out[:n]), list(out[n:])


def _sum_owner(chip_arr, pairs, got, name):
    nb, h, cols = got.shape
    tr = _row_tile(h, 16, 256)

    def body(chip_ref, own_ref, a_ref, b_ref, c_ref, o_ref):
        o_ref[...] = ((own_ref[0].astype(F32) + a_ref[0].astype(F32)) + b_ref[0].astype(F32)) + c_ref[0].astype(F32)

    def slot(off):
        return pl.BlockSpec((1, tr, cols), lambda i, chip_ref: ((chip_ref[0] + off) % N_CHIP, i, 0))

    return pl.pallas_call(
        body, name=name,
        grid_spec=pltpu.PrefetchScalarGridSpec(
            num_scalar_prefetch=1, grid=(h // tr,),
            in_specs=[slot(0), slot(1), slot(2), slot(3)],
            out_specs=pl.BlockSpec((tr, cols), lambda i, chip_ref: (i, 0))),
        out_shape=jax.ShapeDtypeStruct((h, cols), F32),
        compiler_params=_cparams(dimension_semantics=("parallel",)),
    )(chip_arr, pairs, got, got, got)


def _swap_with_sibling(arrs, name):
    n = len(arrs)

    def body(*refs):
        ins, outs = refs[:n], refs[n:2 * n]
        send_sems, recv_sems = refs[2 * n:]
        x, y, c, _ = _place()
        cps = []
        for k in range(n):
            cp = pltpu.make_async_remote_copy(
                src_ref=ins[k], dst_ref=outs[k], send_sem=send_sems.at[k], recv_sem=recv_sems.at[k],
                device_id=(x, y, 1 - c), device_id_type=MESH)
            cp.start()
            cps.append(cp)
        for cp in cps:
            cp.wait()

    return pl.pallas_call(
        body, name=name,
        out_shape=[jax.ShapeDtypeStruct(a.shape, a.dtype) for a in arrs],
        in_specs=[ANY_SPEC] * n, out_specs=[ANY_SPEC] * n,
        scratch_shapes=[pltpu.SemaphoreType.DMA((n,)), pltpu.SemaphoreType.DMA((n,))],
    )(*arrs)


def _row_tile(h, mult=8, cap=128):
    for t in range(cap - cap % mult, mult - 1, -mult):
        if h % t == 0:
            return t
    if mult > 8:
        return _row_tile(h, 8, cap)
    raise ValueError(h)


def _cast_bf16(a, name):
    rows, cols = a.shape
    tr = _row_tile(rows, 16, 256)

    def body(a_ref, o_ref):
        o_ref[...] = a_ref[...].astype(BF16)

    return pl.pallas_call(
        body, name=name, grid=(rows // tr,),
        out_shape=jax.ShapeDtypeStruct(a.shape, BF16),
        in_specs=[pl.BlockSpec((tr, cols), lambda i: (i, 0))],
        out_specs=pl.BlockSpec((tr, cols), lambda i: (i, 0)),
        compiler_params=_cparams(dimension_semantics=("parallel",)),
    )(a)


def _pair_sum(c_arr, full, recv, name):
    nb, rows, cols = full.shape

    def body(c_ref, f_ref, r_ref, o_ref):
        o_ref[...] = (f_ref[...] + r_ref[...]).astype(BF16)

    if _halved_by_rows(full.shape):
        h = rows // 2
        tr = _row_tile(h, 16, 256)
        steps = h // tr
        own = pl.BlockSpec((1, tr, cols), lambda b, i, c_ref: (b, c_ref[0] * steps + i, 0))
        half = pl.BlockSpec((1, tr, cols), lambda b, i, c_ref: (b, i, 0))
    else:
        steps = 1
        own = pl.BlockSpec((1, rows, cols // 2), lambda b, i, c_ref: (b, 0, c_ref[0]))
        half = pl.BlockSpec((1, rows, cols // 2), lambda b, i, c_ref: (b, 0, 0))
    return pl.pallas_call(
        body, name=name,
        grid_spec=pltpu.PrefetchScalarGridSpec(
            num_scalar_prefetch=1, grid=(nb, steps), in_specs=[own, half], out_specs=half),
        out_shape=jax.ShapeDtypeStruct(_half_shape(full.shape), BF16),
        compiler_params=_cparams(dimension_semantics=("parallel", "parallel")),
    )(c_arr, full, recv)


def _sum_chips(got, name):
    nb, h, cols = got.shape
    tr = _row_tile(h, 16, 256)

    def body(g_ref, o_ref):
        g = g_ref[...].astype(F32)
        o_ref[...] = ((g[0] + g[1]) + g[2]) + g[3]

    return pl.pallas_call(
        body, name=name, grid=(h // tr,),
        out_shape=jax.ShapeDtypeStruct((h, cols), F32),
        in_specs=[pl.BlockSpec((nb, tr, cols), lambda i: (0, i, 0))],
        out_specs=pl.BlockSpec((tr, cols), lambda i: (i, 0)),
        compiler_params=_cparams(dimension_semantics=("parallel",)),
    )(got)


def _adam_math(g, w, m, v):
    m1 = ADAM_B1 * m + (1.0 - ADAM_B1) * g
    v1 = ADAM_B2 * v + (1.0 - ADAM_B2) * (g * g)
    m_hat = m1 / (1.0 - ADAM_B1 ** ADAM_STEP)
    v_hat = v1 / (1.0 - ADAM_B2 ** ADAM_STEP)
    delta = -ADAM_LR * (m_hat / (jnp.sqrt(v_hat) + ADAM_EPS) + ADAM_WD * w)
    return delta, m1, v1


def _adamw_halves(c_arr, own, other, w, m, v, name):
    rows, cols = w.shape
    by_rows = own.shape[1] == cols

    def body(c_ref, own_ref, oth_ref, w_ref, m_ref, v_ref, g_out, d_out, m_out, v_out):
        if by_rows:
            g = jnp.where(pl.program_id(0) == c_ref[0], own_ref[...], oth_ref[...])
        else:
            own_, oth_ = own_ref[...], oth_ref[...]
            g = jnp.where(c_ref[0] == 0, jnp.concatenate([own_, oth_], axis=1), jnp.concatenate([oth_, own_], axis=1))
        d, m1, v1 = _adam_math(g, w_ref[...], m_ref[...], v_ref[...])
        g_out[...] = g
        d_out[...] = d
        m_out[...] = m1
        v_out[...] = v1

    if by_rows:
        h = rows // 2
        tr = _row_tile(h)
        steps = h // tr
        grid = (2, steps)
        half_spec = pl.BlockSpec((tr, cols), lambda p, i, c_ref: (i, 0))
        full_spec = pl.BlockSpec((tr, cols), lambda p, i, c_ref: (p * steps + i, 0))
    else:
        tr = _row_tile(rows)
        grid = (1, rows // tr)
        half_spec = pl.BlockSpec((tr, cols // 2), lambda p, i, c_ref: (i, 0))
        full_spec = pl.BlockSpec((tr, cols), lambda p, i, c_ref: (i, 0))
    return pl.pallas_call(
        body, name=name,
        grid_spec=pltpu.PrefetchScalarGridSpec(
            num_scalar_prefetch=1, grid=grid,
            in_specs=[half_spec, half_spec, full_spec, full_spec, full_spec],
            out_specs=[full_spec] * 4),
        out_shape=[jax.ShapeDtypeStruct(w.shape, F32)] * 4,
        compiler_params=_cparams(dimension_semantics=("parallel", "parallel")),
    )(c_arr, own, other, w, m, v)


def _adamw_whole(items, name):
    n = len(items)

    def body(*refs):
        ins, outs = refs[:4 * n], refs[4 * n:]
        for k in range(n):
            g, w, m, v = (r[...] for r in ins[4 * k:4 * k + 4])
            d, m1, v1 = _adam_math(g, w, m, v)
            outs[3 * k][...] = d
            outs[3 * k + 1][...] = m1
            outs[3 * k + 2][...] = v1

    flat = [a for it in items for a in it]
    shapes = [jax.ShapeDtypeStruct(it[1].shape, F32) for it in items for _ in range(3)]
    out = pl.pallas_call(
        body, name=name, out_shape=shapes,
        in_specs=[VMEM_SPEC] * (4 * n), out_specs=[VMEM_SPEC] * (3 * n),
        compiler_params=_cparams(),
    )(*flat)
    return [tuple(out[3 * k:3 * k + 3]) for k in range(n)]


def _adamw_tiled(g, w, m, v, name):
    rows, cols = w.shape
    tr = _row_tile(rows)

    def body(g_ref, w_ref, m_ref, v_ref, d_out, m_out, v_out):
        d, m1, v1 = _adam_math(g_ref[...], w_ref[...], m_ref[...], v_ref[...])
        d_out[...] = d
        m_out[...] = m1
        v_out[...] = v1

    spec = pl.BlockSpec((tr, cols), lambda i: (i, 0))
    return pl.pallas_call(
        body, name=name, grid=(rows // tr,),
        out_shape=[jax.ShapeDtypeStruct(w.shape, F32)] * 3,
        in_specs=[spec] * 4, out_specs=[spec] * 3,
        compiler_params=_cparams(dimension_semantics=("parallel",)),
    )(g, w, m, v)


def _mod_forward(cond, w_mod, b_mod_cols, name):
    def body(c_ref, w_ref, b_ref, o_ref):
        o_ref[...] = _dot(_silu(c_ref[...]), w_ref[...]) + b_ref[...]

    return pl.pallas_call(
        body, name=name, out_shape=jax.ShapeDtypeStruct((cond.shape[0], w_mod.shape[1]), F32),
        in_specs=[VMEM_SPEC] * 3, out_specs=VMEM_SPEC, compiler_params=_cparams(),
    )(cond, w_mod, b_mod_cols)


def _mod_backward(cond, w_mod, dmod_cols, name):
    def body(c_ref, w_ref, d_ref, gw_ref, gc_ref):
        s = _silu(c_ref[...])
        d = d_ref[...]
        gw_ref[...] = _dot_tn(s, d)
        gc_ref[...] = _dot_nt(d[8:16, :], w_ref[...])

    return pl.pallas_call(
        body, name=name,
        out_shape=[jax.ShapeDtypeStruct(w_mod.shape, F32), jax.ShapeDtypeStruct((8, w_mod.shape[0]), F32)],
        in_specs=[VMEM_SPEC] * 3, out_specs=[VMEM_SPEC] * 2, compiler_params=_cparams(),
    )(cond, w_mod, dmod_cols)


def _col_chunks(width, step=512):
    return [(s, min(step, width - s)) for s in range(0, width, step)]


def _w_in_row(p_off):
    if p_off < 9 * HW:
        return p_off
    return 9 * HW if p_off == OFF_LR else p_off + 2 * RANK


def _in_projection(ctx0, x0, modc, modx, pre1, w_t, n_ctx_tiles, name):
    d = x0.shape[1]
    rows = ctx0.shape[0] + x0.shape[0]
    width = P_WIDTH

    def body(ctx_ref, x_ref, modc_ref, modx_ref, pre_ref, w_ref, h_ref, p_ref):
        is_ctx = pl.program_id(0) < n_ctx_tiles
        n, _ = _rms(jnp.where(is_ctx, ctx_ref[...], x_ref[...]))
        shift = jnp.where(is_ctx, modc_ref[0:1, :], modx_ref[0:1, :])
        scale = jnp.where(is_ctx, modc_ref[1:2, :], modx_ref[1:2, :])
        h = (n * pre_ref[...] * (1.0 + scale) + shift).astype(BF16)
        h_ref[...] = h
        for s, w in _col_chunks(width):
            p_ref[:, s:s + w] = _dot_nt(h, w_ref[_w_in_row(s):_w_in_row(s) + w, :])

    row = lambda i: (i, 0)
    fixed = lambda i: (0, 0)
    return pl.pallas_call(
        body, name=name, grid=(rows // TM,),
        out_shape=[jax.ShapeDtypeStruct((rows, d), BF16), jax.ShapeDtypeStruct((rows, width), F32)],
        in_specs=[pl.BlockSpec((TM, d), lambda i: (jnp.minimum(i, n_ctx_tiles - 1), 0)),
                  pl.BlockSpec((TM, d), lambda i: (jnp.maximum(i - n_ctx_tiles, 0), 0)),
                  pl.BlockSpec((8, d), fixed), pl.BlockSpec((8, d), fixed), pl.BlockSpec((1, d), fixed), VMEM_SPEC],
        out_specs=[pl.BlockSpec((TM, d), row), pl.BlockSpec((TM, width), row)],
        compiler_params=_cparams(dimension_semantics=("parallel",)),
    )(ctx0, x0, modc, modx, pre1, w_t)


C_HQ, C_HI, C_HF_FW, C_HF_BW, C_HGATE, C_GQ, C_GK, C_GV, C_GGATE = range(9)
OFF_GATE_HG = 9 * HW
OFF_LR = 13 * HW
P_WIDTH = OFF_LR + 128


def _head_norm_fwd(o, w):
    outs, ns, rs = [], [], []
    for h in range(NH):
        n, r = _rms(o[:, h * HD:(h + 1) * HD])
        ns.append(n)
        rs.append(r)
        outs.append(n * w)
    return jnp.concatenate(outs, axis=1), ns, rs


def _mixer_tail(z, o_hg, o_gla, p_hgate, p_ggate, p_gate_hg, p_gate_gla, hg_on, gla_on, wbh, wbg, wout):
    on_hg, n_hg, r_hg = _head_norm_fwd(o_hg, hg_on)
    on_gla, n_gla, r_gla = _head_norm_fwd(o_gla, gla_on)
    og_hg = (on_hg * _silu(p_hgate)).astype(BF16)
    og_gla = (on_gla * _silu(p_ggate)).astype(BF16)
    b_hg = jnp.dot(og_hg, wbh, preferred_element_type=F32)
    b_gla = jnp.dot(og_gla, wbg, preferred_element_type=F32)
    s_hg = _sigmoid(p_gate_hg)
    s_gla = _sigmoid(p_gate_gla)
    merged = (s_hg * b_hg + s_gla * b_gla).astype(BF16)
    y1 = jnp.dot(merged, wout, preferred_element_type=F32)
    return dict(on_hg=on_hg, n_hg=n_hg, r_hg=r_hg, on_gla=on_gla, n_gla=n_gla, r_gla=r_gla, og_hg=og_hg,
                og_gla=og_gla, b_hg=b_hg, b_gla=b_gla, s_hg=s_hg, s_gla=s_gla, merged=merged, y1=y1)


def _mixer_ffn(x_lat, p, o_list, modx, norms, onorms, w_br_hg, w_br_gla, w_out, w_gate, w_up, w_down, target,
               n_ctx_tiles, name):
    rows, d = x_lat.shape
    dff = w_gate.shape[0]
    inv_d = 1.0 / d

    def body(x_ref, ofw_hg, obw_hg, ofw_gla, obw_gla, p_hgate, p_ggate, p_ghg_a, p_ghg_b, p_ggla_a, p_ggla_b,
             modx_ref, norm_ref, on_ref, wbh_ref, wbg_ref, wout_ref, wg_ref, wu_ref, wd_ref, t_ref,
             loss_ref, dz2_ref, y1_ref, mrg_ref, oghg_ref, oggla_ref, h2_ref, a_ref, du_ref, dv_ref, dy2_ref,
             stat_ref):
        i = pl.program_id(0)
        post1, pre2, post2 = norm_ref[1:2, :], norm_ref[2:3, :], norm_ref[3:4, :]
        gate1, shift2, scale2, gate2 = modx_ref[2:3, :], modx_ref[3:4, :], modx_ref[4:5, :], modx_ref[5:6, :]
        p_gate_hg = jnp.concatenate([p_ghg_a[...], p_ghg_b[...]], axis=1)
        p_gate_gla = jnp.concatenate([p_ggla_a[...], p_ggla_b[...]], axis=1)
        t = _mixer_tail(x_ref[...], ofw_hg[...] + obw_hg[...], ofw_gla[...] + obw_gla[...], p_hgate[...],
                        p_ggate[...], p_gate_hg, p_gate_gla, on_ref[0:1, 0:HD], on_ref[1:2, 0:HD],
                        wbh_ref[...], wbg_ref[...], wout_ref[...])
        y1_ref[...] = t["y1"]
        mrg_ref[...] = t["merged"]
        oghg_ref[...] = t["og_hg"]
        oggla_ref[...] = t["og_gla"]
        n1, _ = _rms(t["y1"])
        z2 = x_ref[...] + n1 * post1 * gate1
        n2, r2 = _rms(z2)
        nw2 = n2 * pre2
        h2 = (nw2 * (1.0 + scale2) + shift2).astype(BF16)
        h2_ref[...] = h2
        u = _dot_nt(h2, wg_ref[...])
        v = _dot_nt(h2, wu_ref[...])
        su = _silu(u)
        a = (su * v).astype(BF16)
        a_ref[...] = a
        y2 = jnp.dot(a, wd_ref[...], preferred_element_type=F32)
        n3, r3 = _rms(y2)
        z3 = z2 + n3 * post2 * gate2
        err = z3 - t_ref[...]
        part = 0.5 * inv_d * jnp.sum(err * err)
        dz3 = err * inv_d
        dgate2 = _colsum(dz3 * n3 * post2)
        tt = dz3 * gate2
        dpost2 = _colsum(tt * n3)
        dy2 = _rms_bwd(tt * post2, n3, r3).astype(BF16)
        dy2_ref[...] = dy2
        da = _dot_nt(dy2, wd_ref[...])
        du = (da * v * _dsilu(u)).astype(BF16)
        dv = (da * su).astype(BF16)
        du_ref[...] = du
        dv_ref[...] = dv
        dh2 = (jnp.dot(du, wg_ref[...], preferred_element_type=F32)
               + jnp.dot(dv, wu_ref[...], preferred_element_type=F32))
        dshift2 = _colsum(dh2)
        dscale2 = _colsum(dh2 * nw2)
        dnw2 = dh2 * (1.0 + scale2)
        dpre2 = _colsum(dnw2 * n2)
        dz2_ref[...] = dz3 + _rms_bwd(dnw2 * pre2, n2, r2)

        @pl.when(i == 0)
        def _():
            stat_ref[...] = jnp.zeros_like(stat_ref)
            loss_ref[...] = jnp.zeros_like(loss_ref)

        for r, val in enumerate((dshift2, dscale2, dgate2, dpre2, dpost2)):
            stat_ref[r:r + 1, :] += val
        loss_ref[...] += part
        stat_ref[5:6, 0:128] += part

    tm = TM_FFN
    ctx_tiles = n_ctx_tiles * (TM // tm)
    lat = lambda i: (i, 0)
    full = lambda i: (i + ctx_tiles, 0)
    fixed = lambda i: (0, 0)

    def pcol(blk):
        return pl.BlockSpec((tm, HW), lambda i: (i + ctx_tiles, blk))

    in_specs = ([pl.BlockSpec((tm, d), lat)] + [pl.BlockSpec((tm, HW), full)] * 4
                + [pcol(C_HGATE), pcol(C_GGATE), pcol(9), pcol(10), pcol(11), pcol(12)]
                + [pl.BlockSpec((8, d), fixed), pl.BlockSpec((8, d), fixed), pl.BlockSpec((8, d), fixed)]
                + [VMEM_SPEC] * 6 + [pl.BlockSpec((tm, d), lat)])
    bf = lambda w: jax.ShapeDtypeStruct((rows, w), BF16)
    out_shape = [jax.ShapeDtypeStruct((8, 128), F32), jax.ShapeDtypeStruct((rows, d), F32),
                 jax.ShapeDtypeStruct((rows, d), F32), bf(d), bf(HW), bf(HW), bf(d), bf(dff), bf(dff), bf(dff), bf(d),
                 jax.ShapeDtypeStruct((8, d), F32)]
    out_specs = [pl.BlockSpec((8, 128), fixed), pl.BlockSpec((tm, d), lat), pl.BlockSpec((tm, d), lat),
                 pl.BlockSpec((tm, d), lat), pl.BlockSpec((tm, HW), lat), pl.BlockSpec((tm, HW), lat),
                 pl.BlockSpec((tm, d), lat), pl.BlockSpec((tm, dff), lat), pl.BlockSpec((tm, dff), lat),
                 pl.BlockSpec((tm, dff), lat), pl.BlockSpec((tm, d), lat), pl.BlockSpec((8, d), fixed)]
    return pl.pallas_call(
        body, name=name, grid=(rows // tm,), out_shape=out_shape, in_specs=in_specs, out_specs=out_specs,
        compiler_params=_cparams(dimension_semantics=("arbitrary",)),
    )(x_lat, *o_list, p, p, p, p, p, p, modx, norms, onorms, w_br_hg, w_br_gla, w_out, w_gate, w_up, w_down, target)


def _mixer_tail_fwd(x_lat, p, o_list, modx, norms, onorms, w_br_hg, w_br_gla, w_out, n_ctx_tiles, name):
    rows, d = x_lat.shape

    def body(x_ref, ofw_hg, obw_hg, ofw_gla, obw_gla, p_hgate, p_ggate, p_ghg_a, p_ghg_b, p_ggla_a, p_ggla_b,
             modx_ref, norm_ref, on_ref, wbh_ref, wbg_ref, wout_ref, z2_ref, y1_ref, mrg_ref, oghg_ref, oggla_ref):
        p_gate_hg = jnp.concatenate([p_ghg_a[...], p_ghg_b[...]], axis=1)
        p_gate_gla = jnp.concatenate([p_ggla_a[...], p_ggla_b[...]], axis=1)
        t = _mixer_tail(x_ref[...], ofw_hg[...] + obw_hg[...], ofw_gla[...] + obw_gla[...], p_hgate[...],
                        p_ggate[...], p_gate_hg, p_gate_gla, on_ref[0:1, 0:HD], on_ref[1:2, 0:HD],
                        wbh_ref[...], wbg_ref[...], wout_ref[...])
        y1_ref[...] = t["y1"]
        mrg_ref[...] = t["merged"]
        oghg_ref[...] = t["og_hg"]
        oggla_ref[...] = t["og_gla"]
        n1, _ = _rms(t["y1"])
        z2_ref[...] = x_ref[...] + n1 * norm_ref[1:2, :] * modx_ref[2:3, :]

    lat = lambda i: (i, 0)
    full = lambda i: (i + n_ctx_tiles, 0)
    fixed = lambda i: (0, 0)

    def pcol(blk):
        return pl.BlockSpec((TM, HW), lambda i: (i + n_ctx_tiles, blk))

    in_specs = ([pl.BlockSpec((TM, d), lat)] + [pl.BlockSpec((TM, HW), full)] * 4
                + [pcol(C_HGATE), pcol(C_GGATE), pcol(9), pcol(10), pcol(11), pcol(12)]
                + [pl.BlockSpec((8, d), fixed)] * 3 + [VMEM_SPEC] * 3)
    bf = lambda w: jax.ShapeDtypeStruct((rows, w), BF16)
    f32 = jax.ShapeDtypeStruct((rows, d), F32)
    return pl.pallas_call(
        body, name=name, grid=(rows // TM,), out_shape=[f32, f32, bf(d), bf(HW), bf(HW)], in_specs=in_specs,
        out_specs=[pl.BlockSpec((TM, d), lat)] * 3 + [pl.BlockSpec((TM, HW), lat)] * 2,
        compiler_params=_cparams(dimension_semantics=("parallel",)),
    )(x_lat, *o_list, p, p, p, p, p, p, modx, norms, onorms, w_br_hg, w_br_gla, w_out)


def _ffn_fwd_bwd(z2, modx, norms, w_gate, w_up, w_down, target, name):
    rows, d = z2.shape
    dff = w_gate.shape[0]
    inv_d = 1.0 / d

    def body(z2_ref, modx_ref, norm_ref, wg_ref, wu_ref, wd_ref, t_ref,
             loss_ref, dz2_ref, h2_ref, a_ref, du_ref, dv_ref, dy2_ref, stat_ref):
        i = pl.program_id(0)
        pre2, post2 = norm_ref[2:3, :], norm_ref[3:4, :]
        shift2, scale2, gate2 = modx_ref[3:4, :], modx_ref[4:5, :], modx_ref[5:6, :]
        z2 = z2_ref[...]
        n2, r2 = _rms(z2)
        nw2 = n2 * pre2
        h2 = (nw2 * (1.0 + scale2) + shift2).astype(BF16)
        h2_ref[...] = h2
        u = _dot_nt(h2, wg_ref[...])
        v = _dot_nt(h2, wu_ref[...])
        su = _silu(u)
        a = (su * v).astype(BF16)
        a_ref[...] = a
        y2 = jnp.dot(a, wd_ref[...], preferred_element_type=F32)
        n3, r3 = _rms(y2)
        err = z2 + n3 * post2 * gate2 - t_ref[...]
        part = 0.5 * inv_d * jnp.sum(err * err)
        dz3 = err * inv_d
        dgate2 = _colsum(dz3 * n3 * post2)
        tt = dz3 * gate2
        dpost2 = _colsum(tt * n3)
        dy2 = _rms_bwd(tt * post2, n3, r3).astype(BF16)
        dy2_ref[...] = dy2
        da = _dot_nt(dy2, wd_ref[...])
        du = (da * v * _dsilu(u)).astype(BF16)
        dv = (da * su).astype(BF16)
        du_ref[...] = du
        dv_ref[...] = dv
        dh2 = (jnp.dot(du, wg_ref[...], preferred_element_type=F32)
               + jnp.dot(dv, wu_ref[...], preferred_element_type=F32))
        dshift2 = _colsum(dh2)
        dscale2 = _colsum(dh2 * nw2)
        dnw2 = dh2 * (1.0 + scale2)
        dpre2 = _colsum(dnw2 * n2)
        dz2_ref[...] = dz3 + _rms_bwd(dnw2 * pre2, n2, r2)

        @pl.when(i == 0)
        def _():
            stat_ref[...] = jnp.zeros_like(stat_ref)
            loss_ref[...] = jnp.zeros_like(loss_ref)

        for r, val in enumerate((dshift2, dscale2, dgate2, dpre2, dpost2)):
            stat_ref[r:r + 1, :] += val
        loss_ref[...] += part
        stat_ref[5:6, 0:128] += part

    lat = lambda i: (i, 0)
    fixed = lambda i: (0, 0)
    bf = lambda w: jax.ShapeDtypeStruct((rows, w), BF16)
    return pl.pallas_call(
        body, name=name, grid=(rows // TM,),
        out_shape=[jax.ShapeDtypeStruct((8, 128), F32), jax.ShapeDtypeStruct((rows, d), F32), bf(d), bf(dff), bf(dff),
                   bf(dff), bf(d), jax.ShapeDtypeStruct((8, d), F32)],
        in_specs=[pl.BlockSpec((TM, d), lat), pl.BlockSpec((8, d), fixed), pl.BlockSpec((8, d), fixed)]
        + [VMEM_SPEC] * 3 + [pl.BlockSpec((TM, d), lat)],
        out_specs=[pl.BlockSpec((8, 128), fixed), pl.BlockSpec((TM, d), lat), pl.BlockSpec((TM, d), lat),
                   pl.BlockSpec((TM, dff), lat), pl.BlockSpec((TM, dff), lat), pl.BlockSpec((TM, dff), lat),
                   pl.BlockSpec((TM, d), lat), pl.BlockSpec((8, d), fixed)],
        compiler_params=_cparams(dimension_semantics=("arbitrary",)),
    )(z2, modx, norms, w_gate, w_up, w_down, target)


def _mixer_tail_bwd(x_lat, p, o_list, dz2, y1, modx, norms, onorms, w_br_hg, w_br_gla, w_out, n_ctx_tiles, n_tiles,
                    name):
    rows, d = x_lat.shape
    total = n_tiles * TM

    def body(x_ref, ofw_hg, obw_hg, ofw_gla, obw_gla, p_hgate, p_ggate, p_ghg_a, p_ghg_b, p_ggla_a, p_ggla_b,
             dz2_ref, y1_ref, modx_ref, norm_ref, on_ref, wbh_ref, wbg_ref, wout_ref,
             dohg_ref, dogla_ref, dhgate_ref, dggate_ref, dghg_ref, dggla_ref, dy1_ref, dbhg_ref, dbgla_ref,
             stat_ref):
        i = pl.program_id(0)

        @pl.when(i == 0)
        def _():
            stat_ref[...] = jnp.zeros_like(stat_ref)

        @pl.when(i < n_ctx_tiles)
        def _():
            for ref in (dohg_ref, dogla_ref, dhgate_ref, dggate_ref, dghg_ref, dggla_ref):
                ref[...] = jnp.zeros_like(ref)

        @pl.when(i >= n_ctx_tiles)
        def _():
            post1, gate1 = norm_ref[1:2, :], modx_ref[2:3, :]
            hg_on, gla_on = on_ref[0:1, 0:HD], on_ref[1:2, 0:HD]
            p_gate_hg = jnp.concatenate([p_ghg_a[...], p_ghg_b[...]], axis=1)
            p_gate_gla = jnp.concatenate([p_ggla_a[...], p_ggla_b[...]], axis=1)
            ph, pg = p_hgate[...], p_ggate[...]
            t = _mixer_tail(x_ref[...], ofw_hg[...] + obw_hg[...], ofw_gla[...] + obw_gla[...], ph, pg,
                            p_gate_hg, p_gate_gla, hg_on, gla_on, wbh_ref[...], wbg_ref[...], wout_ref[...])
            dz2 = dz2_ref[...]
            n1, r1 = _rms(y1_ref[...])
            dgate1 = _colsum(dz2 * n1 * post1)
            tt = dz2 * gate1
            dpost1 = _colsum(tt * n1)
            dy1 = _rms_bwd(tt * post1, n1, r1).astype(BF16)
            dy1_ref[...] = dy1
            dmerged = _dot_nt(dy1, wout_ref[...])
            dghg_ref[...] = dmerged * t["b_hg"] * t["s_hg"] * (1.0 - t["s_hg"])
            dggla_ref[...] = dmerged * t["b_gla"] * t["s_gla"] * (1.0 - t["s_gla"])
            db_hg = (dmerged * t["s_hg"]).astype(BF16)
            db_gla = (dmerged * t["s_gla"]).astype(BF16)
            dbhg_ref[...] = db_hg
            dbgla_ref[...] = db_gla
            don_acc = []
            for (db, wb, pgate, on, ns, rs, gain, gate_ref, do_ref) in (
                    (db_hg, wbh_ref, ph, t["on_hg"], t["n_hg"], t["r_hg"], hg_on, dhgate_ref, dohg_ref),
                    (db_gla, wbg_ref, pg, t["on_gla"], t["n_gla"], t["r_gla"], gla_on, dggate_ref, dogla_ref)):
                dog = _dot_nt(db, wb[...])
                gate_ref[...] = dog * on * _dsilu(pgate)
                don = dog * _silu(pgate)
                acc = jnp.zeros((1, HD), F32)
                for h in range(NH):
                    sl = slice(h * HD, (h + 1) * HD)
                    acc = acc + _colsum(don[:, sl] * ns[h])
                    do_ref[:, sl] = _rms_bwd(don[:, sl] * gain, ns[h], rs[h])
                don_acc.append(acc)
            stat_ref[0:1, :] += dgate1
            stat_ref[1:2, :] += dpost1
            stat_ref[2:3, 0:HD] += don_acc[0]
            stat_ref[2:3, HD:2 * HD] += don_acc[1]

    lat = lambda i: (jnp.maximum(i - n_ctx_tiles, 0), 0)
    full = lambda i: (i, 0)
    fixed = lambda i: (0, 0)

    def pcol(blk):
        return pl.BlockSpec((TM, HW), lambda i: (i, blk))

    in_specs = ([pl.BlockSpec((TM, d), lat)] + [pl.BlockSpec((TM, HW), full)] * 4
                + [pcol(C_HGATE), pcol(C_GGATE), pcol(9), pcol(10), pcol(11), pcol(12)]
                + [pl.BlockSpec((TM, d), lat), pl.BlockSpec((TM, d), lat)]
                + [pl.BlockSpec((8, d), fixed)] * 3 + [VMEM_SPEC] * 3)
    f = lambda w: jax.ShapeDtypeStruct((total, w), F32)
    out_shape = [f(HW), f(HW), f(HW), f(HW), f(d), f(d), jax.ShapeDtypeStruct((rows, d), BF16),
                 jax.ShapeDtypeStruct((rows, d), BF16), jax.ShapeDtypeStruct((rows, d), BF16),
                 jax.ShapeDtypeStruct((8, d), F32)]
    out_specs = ([pl.BlockSpec((TM, HW), full)] * 4 + [pl.BlockSpec((TM, d), full)] * 2
                 + [pl.BlockSpec((TM, d), lat)] * 3 + [pl.BlockSpec((8, d), fixed)])
    return pl.pallas_call(
        body, name=name, grid=(n_tiles,), out_shape=out_shape, in_specs=in_specs, out_specs=out_specs,
        compiler_params=_cparams(dimension_semantics=("arbitrary",)),
    )(x_lat, *o_list, p, p, p, p, p, p, dz2, y1, modx, norms, onorms, w_br_hg, w_br_gla, w_out)


def _in_projection_bwd(ctx0, x0, dz2, modc, modx, pre1, w_t, pieces, n_ctx_tiles, name):
    d = x0.shape[1]
    rows = ctx0.shape[0] + x0.shape[0]
    lat_rows = dz2.shape[0]
    width = P_WIDTH
    n_pieces = len(pieces)

    def body(*refs):
        ctx_ref, x_ref, dz2_ref, modc_ref, modx_ref, pre_ref, w_ref = refs[:7]
        (dhq_f, dhq_b, dhi_f, dhi_b, dhf_f, dhf_b, dhgate, dgq_f, dgq_b, dgk_f, dgk_b, dgv_f, dgv_b, dggate,
         dghg, dggla, dlr_f, dlr_b) = refs[7:7 + n_pieces]
        dp_ref, gx_ref, stat_ref = refs[7 + n_pieces:]
        i = pl.program_id(0)
        is_ctx = i < n_ctx_tiles
        z = jnp.where(is_ctx, ctx_ref[...], x_ref[...])
        sections = [
            (0, dhq_f[...] + dhq_b[...]), (HW, dhi_f[...] + dhi_b[...]), (2 * HW, dhf_f[...]), (3 * HW, dhf_b[...]),
            (4 * HW, dhgate[...]), (5 * HW, dgq_f[...] + dgq_b[...]), (6 * HW, dgk_f[...] + dgk_b[...]),
            (7 * HW, dgv_f[...] + dgv_b[...]), (8 * HW, dggate[...]),
            (9 * HW, dghg[:, 0:HW]), (10 * HW, dghg[:, HW:2 * HW]),
            (11 * HW, dggla[:, 0:HW]), (12 * HW, dggla[:, HW:2 * HW]), (OFF_LR, dlr_f[...] + dlr_b[...])]
        dh = jnp.zeros((TM, d), F32)
        for off, val in sections:
            w = val.shape[1]
            vb = val.astype(BF16)
            dp_ref[:, off:off + w] = vb
            dh = dh + jnp.dot(vb, w_ref[_w_in_row(off):_w_in_row(off) + w, :], preferred_element_type=F32)
        n, r = _rms(z)
        pre = pre_ref[...]
        scale = jnp.where(is_ctx, modc_ref[1:2, :], modx_ref[1:2, :])
        nw = n * pre
        dshift = _colsum(dh)
        dscale = _colsum(dh * nw)
        dnw = dh * (1.0 + scale)
        dpre = _colsum(dnw * n)
        gx_ref[...] = dz2_ref[...] + _rms_bwd(dnw * pre, n, r)
        zero = jnp.zeros((1, d), F32)

        @pl.when(i == 0)
        def _():
            stat_ref[...] = jnp.zeros_like(stat_ref)

        stat_ref[0:1, :] += jnp.where(is_ctx, zero, dshift)
        stat_ref[1:2, :] += jnp.where(is_ctx, zero, dscale)
        stat_ref[2:3, :] += jnp.where(is_ctx, dshift, zero)
        stat_ref[3:4, :] += jnp.where(is_ctx, dscale, zero)
        stat_ref[4:5, :] += dpre

    full = lambda i: (i, 0)
    lat = lambda i: (jnp.maximum(i - n_ctx_tiles, 0), 0)
    fixed = lambda i: (0, 0)
    piece_specs = [pl.BlockSpec((TM, a.shape[1]), full) for a in pieces]
    in_specs = [pl.BlockSpec((TM, d), lambda i: (jnp.minimum(i, n_ctx_tiles - 1), 0)), pl.BlockSpec((TM, d), lat),
                pl.BlockSpec((TM, d), lat), pl.BlockSpec((8, d), fixed),
                pl.BlockSpec((8, d), fixed), pl.BlockSpec((1, d), fixed), VMEM_SPEC] + piece_specs
    return pl.pallas_call(
        body, name=name, grid=(rows // TM,),
        out_shape=[jax.ShapeDtypeStruct((rows, width), BF16), jax.ShapeDtypeStruct((lat_rows, d), F32),
                   jax.ShapeDtypeStruct((8, d), F32)],
        in_specs=in_specs,
        out_specs=[pl.BlockSpec((TM, width), full), pl.BlockSpec((TM, d), lat), pl.BlockSpec((8, d), fixed)],
        compiler_params=_cparams(dimension_semantics=("arbitrary",)),
    )(ctx0, x0, dz2, modc, modx, pre1, w_t, *pieces)


def _transposed_lhs_matmul(x_ref, dy_ref, o_ref, xt_ref):
    @pl.when(pl.program_id(1) == 0)
    def _():
        xt_ref[...] = x_ref[...].T

    o_ref[...] = jnp.dot(xt_ref[...], dy_ref[...], preferred_element_type=F32)


def _w_in_grad(dp, h1, n_cols, name):
    rows, d = h1.shape
    n_main = OFF_LR // HW
    lr0 = _w_in_row(OFF_LR)

    def body(x_ref, xlr_ref, h_ref, o_hbm, xt_ref, acc_ref, sem):
        i = pl.program_id(0)

        def main_copy(step):
            row = jnp.where(step < 9, step * HW, step * HW + 2 * RANK)
            return pltpu.make_async_copy(acc_ref, o_hbm.at[pl.ds(pl.multiple_of(row, 8), HW), :], sem)

        lr_copy = pltpu.make_async_copy(acc_ref.at[0:2 * RANK, :], o_hbm.at[lr0:lr0 + 2 * RANK, :], sem)

        @pl.when(i < n_main)
        def _():
            xt_ref[...] = x_ref[...].T

        @pl.when(i > 0)
        def _():
            main_copy(i - 1).wait()

        @pl.when(i < n_main)
        def _():
            acc_ref[...] = jnp.dot(xt_ref[...], h_ref[...], preferred_element_type=F32)
            main_copy(i).start()

        @pl.when(i == n_main)
        def _():
            xt_ref[0:128, :] = xlr_ref[...].T
            acc_ref[0:128, :] = jnp.dot(xt_ref[0:128, :], h_ref[...], preferred_element_type=F32)
            lr_copy.start()
            lr_copy.wait()

    return pl.pallas_call(
        body, name=name, grid=(n_main + 1,),
        out_shape=jax.ShapeDtypeStruct((n_cols, d), F32),
        in_specs=[pl.BlockSpec((rows, HW), lambda i: (0, jnp.minimum(i, n_main - 1))),
                  pl.BlockSpec((rows, 128), lambda i: (0, OFF_LR // 128)),
                  pl.BlockSpec((rows, d), lambda i: (0, 0))],
        out_specs=ANY_SPEC,
        scratch_shapes=[pltpu.VMEM((HW, rows), BF16), pltpu.VMEM((HW, d), F32), pltpu.SemaphoreType.DMA],
        compiler_params=_cparams(dimension_semantics=("arbitrary",)),
    )(dp, dp, h1)


def _weight_grad(xs, dy, name, tk=None, tn=512, k_first=0, k_tiles=None):
    rows = dy.shape[0]
    n = dy.shape[1]
    tn_ = min(tn, n)
    tk_ = xs.shape[1] if tk is None else tk
    k_tiles = xs.shape[1] // tk_ if k_tiles is None else k_tiles
    k = k_tiles * tk_

    return pl.pallas_call(
        functools.partial(_transposed_lhs_matmul), name=name, grid=(k_tiles, n // tn_),
        out_shape=jax.ShapeDtypeStruct((k, n), F32),
        in_specs=[pl.BlockSpec((rows, tk_), lambda i, j: (0, i + k_first)),
                  pl.BlockSpec((rows, tn_), lambda i, j: (0, j))],
        out_specs=pl.BlockSpec((tk_, tn_), lambda i, j: (i, j)),
        scratch_shapes=[pltpu.VMEM((tk_, rows), BF16)],
        compiler_params=_cparams(dimension_semantics=("parallel", "arbitrary")),
    )(xs, dy)


def _running_sum(x, fw):
    c = x.shape[0]
    row = lax.broadcasted_iota(jnp.int32, (c, 1), 0)
    s = 1
    while s < c:
        if fw:
            x = x + jnp.where(row >= s, pltpu.roll(x, s, axis=0), 0.0)
        else:
            x = x + jnp.where(row < c - s, pltpu.roll(x, c - s, axis=0), 0.0)
        s *= 2
    return x


def _chunk_terms(q, k, g, fw):
    c = CHUNK
    r = lax.broadcasted_iota(jnp.int32, (c, c), 0)
    s = lax.broadcasted_iota(jnp.int32, (c, c), 1)
    causal = (s <= r) if fw else (s >= r)
    causal_t = (s >= r) if fw else (s <= r)
    cum = _running_sum(g, fw)
    row = lax.broadcasted_iota(jnp.int32, (c, 1), 0)
    pos = row if fw else (c - 1 - row)
    starts = [None]
    for j in range(1, NSUB):
        rj = SUB * j - 1 if fw else c - SUB * j
        starts.append(cum[rj:rj + 1, :])
    in_blk = [(pos >= SUB * j) & (pos < SUB * (j + 1)) for j in range(NSUB)]
    e = [jnp.exp(cum)]
    for j in range(1, NSUB):
        e.append(jnp.exp(jnp.where(pos >= SUB * j, cum - starts[j], -1e30)))
    own = jnp.zeros_like(cum)
    for j in range(1, NSUB):
        own = own + jnp.where(in_blk[j], starts[j], 0.0)
    kscale = jnp.exp(own - cum)
    rend = c - 1 if fw else 0
    cend = cum[rend:rend + 1, :]
    tail = jnp.exp(cend - cum)
    qcat = jnp.concatenate([q * e[j] for j in range(NSUB)], axis=1).astype(BF16)
    kt = k * kscale
    km = jnp.concatenate([jnp.where(in_blk[j], kt, 0.0) for j in range(NSUB)], axis=1).astype(BF16)
    return dict(causal=causal, causal_t=causal_t, e=e, in_blk=in_blk, kscale=kscale, cend=cend, tail=tail,
                qcat=qcat, km=km)


def _chunk_fwd(q, k, v, g, st0, fw):
    t = _chunk_terms(q, k, g, fw)
    a = jnp.where(t["causal"], _dot_nt(t["qcat"], t["km"]), 0.0)
    o = _dot(a, v) + _dot_nt(t["qcat"][:, 0:HD], st0)
    st1 = st0 * jnp.exp(t["cend"]) + _dot_tn(v, k * t["tail"])
    return o, st1


def _chunk_bwd(q, k, v, g, st0, do, dst1, fw):
    t = _chunk_terms(q, k, g, fw)
    qcat, km, e = t["qcat"], t["km"], t["e"]
    a_t = jnp.where(t["causal_t"], _dot_nt(km, qcat), 0.0)
    ktail = k * t["tail"]
    dv = _dot(a_t, do) + _dot_nt(ktail, dst1)
    da = jnp.where(t["causal"], _dot_nt(do, v), 0.0)
    da_t = jnp.where(t["causal_t"], _dot_nt(v, do), 0.0)
    dqcat = _dot(da, km)
    dq_inter = e[0] * _dot(do, st0)
    dq = dq_inter
    for j in range(NSUB):
        dq = dq + e[j] * dqcat[:, j * HD:(j + 1) * HD]
    dkm = _dot(da_t, qcat)
    dkt = jnp.zeros_like(k)
    for j in range(NSUB):
        dkt = dkt + jnp.where(t["in_blk"][j], dkm[:, j * HD:(j + 1) * HD], 0.0)
    dk_inter = _dot(v, dst1) * t["tail"]
    dk = dkt * t["kscale"] + dk_inter
    dcum = q * dq_inter - k * dk_inter
    for j in range(NSUB):
        sl = slice(j * HD, (j + 1) * HD)
        dcum = dcum + qcat[:, sl].astype(F32) * dqcat[:, sl] - km[:, sl].astype(F32) * dkm[:, sl]
    ecend = jnp.exp(t["cend"])
    end = ecend * _colsum(st0 * dst1) + _colsum(k * dk_inter)
    dg = _running_sum(dcum, not fw) + end
    dst0 = dst1 * ecend + _dot_tn(do, q * e[0])
    return dq, dk, dv, dg, dst0


def _running_sums(xs, fws):
    c = xs[0].shape[0]
    row = lax.broadcasted_iota(jnp.int32, (c, 1), 0)
    s = 1
    while s < c:
        xs = [x + (jnp.where(row >= s, pltpu.roll(x, s, axis=0), 0.0) if fw else
                   jnp.where(row < c - s, pltpu.roll(x, c - s, axis=0), 0.0)) for x, fw in zip(xs, fws)]
        s *= 2
    return xs


def _chunks_terms(qs, ks, gs, fws):
    c = CHUNK
    n = len(qs)
    r = lax.broadcasted_iota(jnp.int32, (c, c), 0)
    s = lax.broadcasted_iota(jnp.int32, (c, c), 1)
    row = lax.broadcasted_iota(jnp.int32, (c, 1), 0)
    per_dir = {}
    for fw in set(fws):
        pos = row if fw else (c - 1 - row)
        per_dir[fw] = dict(
            causal=(s <= r) if fw else (s >= r), causal_t=(s >= r) if fw else (s <= r), pos=pos,
            in_blk=[(pos >= SUB * j) & (pos < SUB * (j + 1)) for j in range(NSUB)],
            start_row=[None] + [SUB * j - 1 if fw else c - SUB * j for j in range(1, NSUB)],
            rend=c - 1 if fw else 0)
    dirs = [per_dir[fw] for fw in fws]
    cums = _running_sums(gs, fws)
    starts = [[None] + [cum[d["start_row"][j]:d["start_row"][j] + 1, :] for j in range(1, NSUB)]
              for cum, d in zip(cums, dirs)]
    es = [[jnp.exp(cum) for cum in cums]]
    for j in range(1, NSUB):
        es.append([jnp.exp(jnp.where(d["pos"] >= SUB * j, cum - st[j], -1e30)) for cum, st, d in zip(cums, starts, dirs)])
    owns = [sum(jnp.where(d["in_blk"][j], st[j], 0.0) for j in range(1, NSUB)) for st, d in zip(starts, dirs)]
    kscales = [jnp.exp(own - cum) for own, cum in zip(owns, cums)]
    cends = [cum[d["rend"]:d["rend"] + 1, :] for cum, d in zip(cums, dirs)]
    tails = [jnp.exp(cend - cum) for cend, cum in zip(cends, cums)]
    qcats = [jnp.concatenate([q * es[j][i] for j in range(NSUB)], axis=1).astype(BF16) for i, q in enumerate(qs)]
    kts = [k * ksc for k, ksc in zip(ks, kscales)]
    kms = [jnp.concatenate([jnp.where(d["in_blk"][j], kt, 0.0) for j in range(NSUB)], axis=1).astype(BF16)
           for kt, d in zip(kts, dirs)]
    e_by_lane = [[es[j][i] for j in range(NSUB)] for i in range(n)]
    return dict(dirs=dirs, e=e_by_lane, kscale=kscales, cend=cends, tail=tails, qcat=qcats, km=kms)


def _chunks_fwd(qs, ks, vs, gs, st0s, fws):
    t = _chunks_terms(qs, ks, gs, fws)
    scores = [_dot_nt(qc, km) for qc, km in zip(t["qcat"], t["km"])]
    a = [jnp.where(d["causal"], sc, 0.0) for sc, d in zip(scores, t["dirs"])]
    inter = [_dot_nt(qc[:, 0:HD], st0) for qc, st0 in zip(t["qcat"], st0s)]
    intra = [_dot(a_, v) for a_, v in zip(a, vs)]
    os_ = [x + y for x, y in zip(intra, inter)]
    upd = [_dot_tn(v, k * tl) for v, k, tl in zip(vs, ks, t["tail"])]
    st1s = [st0 * jnp.exp(ce) + u for st0, ce, u in zip(st0s, t["cend"], upd)]
    return os_, st1s


def _chunks_bwd(qs, ks, vs, gs, st0s, dos, dst1s, fws):
    n = len(qs)
    t = _chunks_terms(qs, ks, gs, fws)
    qcat, km, e, dirs = t["qcat"], t["km"], t["e"], t["dirs"]
    a_t = [jnp.where(d["causal_t"], _dot_nt(km_, qc), 0.0) for km_, qc, d in zip(km, qcat, dirs)]
    ktail = [k * tl for k, tl in zip(ks, t["tail"])]
    dv_a = [_dot(at, do) for at, do in zip(a_t, dos)]
    dv_b = [_dot_nt(kt, ds) for kt, ds in zip(ktail, dst1s)]
    dv = [x + y for x, y in zip(dv_a, dv_b)]
    da = [jnp.where(d["causal"], _dot_nt(do, v), 0.0) for do, v, d in zip(dos, vs, dirs)]
    da_t = [jnp.where(d["causal_t"], _dot_nt(v, do), 0.0) for do, v, d in zip(dos, vs, dirs)]
    dqcat = [_dot(da_, km_) for da_, km_ in zip(da, km)]
    dq_inter = [e[i][0] * _dot(dos[i], st0s[i]) for i in range(n)]
    dkm = [_dot(dat, qc) for dat, qc in zip(da_t, qcat)]
    dk_inter = [_dot(v, ds) * tl for v, ds, tl in zip(vs, dst1s, t["tail"])]
    dq = [dq_inter[i] + sum(e[i][j] * dqcat[i][:, j * HD:(j + 1) * HD] for j in range(NSUB)) for i in range(n)]
    dkt = [sum(jnp.where(dirs[i]["in_blk"][j], dkm[i][:, j * HD:(j + 1) * HD], 0.0) for j in range(NSUB))
           for i in range(n)]
    dk = [dkt[i] * t["kscale"][i] + dk_inter[i] for i in range(n)]
    dcum = [qs[i] * dq_inter[i] - ks[i] * dk_inter[i]
            + sum(qcat[i][:, j * HD:(j + 1) * HD].astype(F32) * dqcat[i][:, j * HD:(j + 1) * HD]
                  - km[i][:, j * HD:(j + 1) * HD].astype(F32) * dkm[i][:, j * HD:(j + 1) * HD] for j in range(NSUB))
            for i in range(n)]
    ecend = [jnp.exp(ce) for ce in t["cend"]]
    end = [ecend[i] * _colsum(st0s[i] * dst1s[i]) + _colsum(ks[i] * dk_inter[i]) for i in range(n)]
    sums = _running_sums(dcum, [not fw for fw in fws])
    dg = [sm + en for sm, en in zip(sums, end)]
    upd = [_dot_tn(dos[i], qs[i] * e[i][0]) for i in range(n)]
    dst0 = [dst1s[i] * ecend[i] + upd[i] for i in range(n)]
    return dq, dk, dv, dg, dst0


def _chunk_index(step, n_ctx_chunks, n_chunks, fw):
    if fw:
        return step
    return jnp.where(step < n_ctx_chunks, n_ctx_chunks - 1 - step, n_chunks - 1 + n_ctx_chunks - step)


def _hg_inputs(hq, hf, lbv, d_idx, sl):
    lb = _sigmoid(lbv[d_idx:d_idx + 1, sl] - lbv[2 + d_idx:3 + d_idx, sl])
    sg = _sigmoid(hf)
    f = lb + (1.0 - lb) * sg
    return _silu(hq), 1.0 - f, jnp.log(f), f, sg, lb


def _scan_fwd(p, side, n_ctx_chunks, fw, branch, name):
    rows = p.shape[0]
    n_chunks = rows // CHUNK
    d_idx = 0 if fw else 1
    hg = branch == "hg"
    cols = (C_HQ, C_HI, C_HF_FW + d_idx) if hg else (C_GQ, C_GK, C_GV)

    def body(*refs):
        if hg:
            a_ref, b_ref, c_ref, lb_ref, o_ref, st_ref, state = refs
        else:
            a_ref, b_ref, c_ref, lr_ref, wgk_ref, bgk_ref, o_ref, st_ref, state = refs
            logits = _dot(lr_ref[...], wgk_ref[...]) + bgk_ref[...]
            g_all = _log_sigmoid(logits) * (1.0 / GATE_NORM)

        @pl.when(pl.program_id(0) == 0)
        def _():
            state[...] = jnp.zeros_like(state)

        for h in range(NH):
            sl = slice(h * HD, (h + 1) * HD)
            if hg:
                q, k, g, _, _, _ = _hg_inputs(a_ref[:, sl], c_ref[:, sl], lb_ref[...], d_idx, sl)
                v = b_ref[:, sl]
            else:
                q, k, v, g = a_ref[:, sl] * (HD ** -0.5), b_ref[:, sl], c_ref[:, sl], g_all[:, sl]
            st0 = state[h]
            st_ref[0, h] = st0
            o, st1 = _chunk_fwd(q, k, v, g, st0, fw)
            o_ref[:, sl] = o
            state[h] = st1

    def cmap(blk):
        return pl.BlockSpec((CHUNK, HW), lambda j: (_chunk_index(j, n_ctx_chunks, n_chunks, fw), blk))

    fixed = lambda j: (0, 0)
    in_specs = [cmap(cols[0]), cmap(cols[1]), cmap(cols[2])]
    if hg:
        in_specs += [pl.BlockSpec((4, HW), fixed)]
        args = (p, p, p, side)
    else:
        in_specs += [pl.BlockSpec((CHUNK, 128), lambda j: (_chunk_index(j, n_ctx_chunks, n_chunks, fw), OFF_LR // 128)),
                     pl.BlockSpec((128, HW), fixed), pl.BlockSpec((1, HW), fixed)]
        args = (p, p, p, p, side[0], side[1])
    return pl.pallas_call(
        body, name=name, grid=(n_chunks,),
        out_shape=[jax.ShapeDtypeStruct((rows, HW), F32), jax.ShapeDtypeStruct((n_chunks, NH, HD, HD), F32)],
        in_specs=in_specs,
        out_specs=[pl.BlockSpec((CHUNK, HW), lambda j: (_chunk_index(j, n_ctx_chunks, n_chunks, fw), 0)),
                   pl.BlockSpec((1, NH, HD, HD), lambda j: (_chunk_index(j, n_ctx_chunks, n_chunks, fw), 0, 0, 0))],
        scratch_shapes=[pltpu.VMEM((NH, HD, HD), F32)],
        compiler_params=_cparams(dimension_semantics=("arbitrary",)),
    )(*args)


def _scan_fwd_both(p, side, n_ctx_chunks, branch, name):
    rows = p.shape[0]
    n_chunks = rows // CHUNK
    hg = branch == "hg"
    n_in = 4 if hg else 6

    def body(*refs):
        ins, outs, state = refs[:2 * n_in], refs[2 * n_in:2 * n_in + 4], refs[-1]

        @pl.when(pl.program_id(0) == 0)
        def _():
            state[...] = jnp.zeros_like(state)

        lanes, where = [], []
        for di, fw in enumerate((True, False)):
            r = ins[di * n_in:(di + 1) * n_in]
            o_ref, st_ref = outs[2 * di], outs[2 * di + 1]
            if hg:
                a_ref, b_ref, c_ref, lb_ref = r
            else:
                a_ref, b_ref, c_ref, lr_ref, wgk_ref, bgk_ref = r
                logits = _dot(lr_ref[...], wgk_ref[...]) + bgk_ref[...]
                g_all = _log_sigmoid(logits) * (1.0 / GATE_NORM)
            for h in range(NH):
                sl = slice(h * HD, (h + 1) * HD)
                if hg:
                    q, k, g, _, _, _ = _hg_inputs(a_ref[:, sl], c_ref[:, sl], lb_ref[...], di, sl)
                    v = b_ref[:, sl]
                else:
                    q, k, v, g = a_ref[:, sl] * (HD ** -0.5), b_ref[:, sl], c_ref[:, sl], g_all[:, sl]
                lanes.append((q, k, v, g, state[di, h], fw))
                where.append((di, h, sl, o_ref, st_ref))
        qs, ks, vs, gs, st0s, fws = (list(col) for col in zip(*lanes))
        os_, st1s = _chunks_fwd(qs, ks, vs, gs, st0s, fws)
        for (di, h, sl, o_ref, st_ref), st0, o, st1 in zip(where, st0s, os_, st1s):
            st_ref[0, h] = st0
            o_ref[:, sl] = o
            state[di, h] = st1

    fixed = lambda j: (0, 0)
    in_specs, args, out_specs = [], [], []
    for di, fw in enumerate((True, False)):
        chunk = functools.partial(_chunk_index, n_ctx_chunks=n_ctx_chunks, n_chunks=n_chunks, fw=fw)

        def cmap(blk, width=HW, chunk=chunk):
            return pl.BlockSpec((CHUNK, width), lambda j: (chunk(j), blk))

        if hg:
            in_specs += [cmap(C_HQ), cmap(C_HI), cmap(C_HF_FW + di), pl.BlockSpec((4, HW), fixed)]
            args += [p, p, p, side]
        else:
            in_specs += [cmap(C_GQ), cmap(C_GK), cmap(C_GV), cmap(OFF_LR // 128, 128),
                         pl.BlockSpec((128, HW), fixed), pl.BlockSpec((1, HW), fixed)]
            args += [p, p, p, p, side[di][0], side[di][1]]
        out_specs += [cmap(0), pl.BlockSpec((1, NH, HD, HD), lambda j, chunk=chunk: (chunk(j), 0, 0, 0))]
    return pl.pallas_call(
        body, name=name, grid=(n_chunks,),
        out_shape=[jax.ShapeDtypeStruct((rows, HW), F32), jax.ShapeDtypeStruct((n_chunks, NH, HD, HD), F32)] * 2,
        in_specs=in_specs, out_specs=out_specs,
        scratch_shapes=[pltpu.VMEM((2, NH, HD, HD), F32)],
        compiler_params=_cparams(dimension_semantics=("arbitrary",)),
    )(*args)


def _scan_bwd_both(p, side, states, d_o, n_ctx_chunks, branch, name):
    rows = p.shape[0]
    n_chunks = rows // CHUNK
    hg = branch == "hg"
    n_in = 6 if hg else 8
    n_out = 4 if hg else 6

    def body(*refs):
        ins, outs, dstate = refs[:2 * n_in], refs[2 * n_in:2 * n_in + 2 * n_out], refs[-1]
        first = pl.program_id(0) == 0

        @pl.when(first)
        def _():
            dstate[...] = jnp.zeros_like(dstate)

        lanes, where, extra, ctx = [], [], [], []
        for di, fw in enumerate((True, False)):
            r, w = ins[di * n_in:(di + 1) * n_in], outs[di * n_out:(di + 1) * n_out]
            if hg:
                a_ref, b_ref, c_ref, lb_ref, st_ref, do_ref = r
                da_ref, db_ref, dc_ref, dlb_ref = w
                acc_refs = (dlb_ref,)
            else:
                a_ref, b_ref, c_ref, lr_ref, wgk_ref, bgk_ref, st_ref, do_ref = r
                da_ref, db_ref, dc_ref, dlr_ref, dwgk_ref, dbias_ref = w
                acc_refs = (dwgk_ref, dbias_ref)
                lr = lr_ref[...]
                logits = _dot(lr, wgk_ref[...]) + bgk_ref[...]
                g_all = _log_sigmoid(logits) * (1.0 / GATE_NORM)

            @pl.when(first)
            def _(acc_refs=acc_refs):
                for ref in acc_refs:
                    ref[...] = jnp.zeros_like(ref)

            for h in range(NH):
                sl = slice(h * HD, (h + 1) * HD)
                if hg:
                    hq, hf = a_ref[:, sl], c_ref[:, sl]
                    q, k, g, f, sg, lb = _hg_inputs(hq, hf, lb_ref[...], di, sl)
                    v = b_ref[:, sl]
                    extra.append((hq, f, sg, lb))
                else:
                    q, k, v, g = a_ref[:, sl] * (HD ** -0.5), b_ref[:, sl], c_ref[:, sl], g_all[:, sl]
                    extra.append(None)
                lanes.append((q, k, v, g, st_ref[0, h], do_ref[:, sl], dstate[di, h], fw))
                where.append((di, h, sl))
            ctx.append((w, None if hg else (lr, logits, wgk_ref)))

        qs, ks, vs, gs, st0s, dos, dst1s, fws = (list(col) for col in zip(*lanes))
        dqs, dks, dvs, dgs, dst0s = _chunks_bwd(qs, ks, vs, gs, st0s, dos, dst1s, fws)
        dg_parts = {0: [], 1: []}
        for (di, h, sl), ex, dq, dk, dv, dg, dst0 in zip(where, extra, dqs, dks, dvs, dgs, dst0s):
            dstate[di, h] = dst0
            w = ctx[di][0]
            if hg:
                hq, f, sg, lb = ex
                da_ref, db_ref, dc_ref, dlb_ref = w
                da_ref[:, sl] = dq * _dsilu(hq)
                db_ref[:, sl] = dv
                df = dg / f - dk
                dc_ref[:, sl] = df * (1.0 - lb) * sg * (1.0 - sg)
                dlb_ref[0:1, sl] += _colsum(df * (1.0 - sg))
            else:
                da_ref, db_ref, dc_ref = w[:3]
                da_ref[:, sl] = dq * (HD ** -0.5)
                db_ref[:, sl] = dk
                dc_ref[:, sl] = dv
                dg_parts[di].append(dg)
        if not hg:
            for di in range(2):
                dlr_ref, dwgk_ref, dbias_ref = ctx[di][0][3:]
                lr, logits, wgk_ref = ctx[di][1]
                dlogits = jnp.concatenate(dg_parts[di], axis=1) * (1.0 / GATE_NORM) * (1.0 - _sigmoid(logits))
                dlr_ref[...] = _dot_nt(dlogits, wgk_ref[...])
                dwgk_ref[...] += _dot_tn(lr, dlogits)
                dbias_ref[0:1, :] += _colsum(dlogits)

    fixed = lambda j: (0, 0)
    big = jax.ShapeDtypeStruct((rows, HW), F32)
    in_specs, args, out_shape, out_specs = [], [], [], []
    for di, fw in enumerate((True, False)):
        def chunk_of(j, fw=fw):
            return _chunk_index(n_chunks - 1 - j, n_ctx_chunks, n_chunks, fw)

        def cmap(blk, width=HW, chunk_of=chunk_of):
            return pl.BlockSpec((CHUNK, width), lambda j: (chunk_of(j), blk))

        st_spec = pl.BlockSpec((1, NH, HD, HD), lambda j, chunk_of=chunk_of: (chunk_of(j), 0, 0, 0))
        if hg:
            in_specs += [cmap(C_HQ), cmap(C_HI), cmap(C_HF_FW + di), pl.BlockSpec((4, HW), fixed), st_spec, cmap(0)]
            args += [p, p, p, side, states[di], d_o]
            out_shape += [big, big, big, jax.ShapeDtypeStruct((8, HW), F32)]
            out_specs += [cmap(0), cmap(0), cmap(0), pl.BlockSpec((8, HW), fixed)]
        else:
            in_specs += [cmap(C_GQ), cmap(C_GK), cmap(C_GV), cmap(OFF_LR // 128, 128),
                         pl.BlockSpec((128, HW), fixed), pl.BlockSpec((1, HW), fixed), st_spec, cmap(0)]
            args += [p, p, p, p, side[di][0], side[di][1], states[di], d_o]
            out_shape += [big, big, big, jax.ShapeDtypeStruct((rows, 128), F32),
                          jax.ShapeDtypeStruct((128, HW), F32), jax.ShapeDtypeStruct((8, HW), F32)]
            out_specs += [cmap(0), cmap(0), cmap(0), cmap(0, 128), pl.BlockSpec((128, HW), fixed),
                          pl.BlockSpec((8, HW), fixed)]
    return pl.pallas_call(
        body, name=name, grid=(n_chunks,), out_shape=out_shape, in_specs=in_specs, out_specs=out_specs,
        scratch_shapes=[pltpu.VMEM((2, NH, HD, HD), F32)],
        compiler_params=_cparams(dimension_semantics=("arbitrary",)),
    )(*args)


def _scan_bwd(p, side, states, d_o, n_ctx_chunks, fw, branch, name):
    rows = p.shape[0]
    n_chunks = rows // CHUNK
    d_idx = 0 if fw else 1
    hg = branch == "hg"
    cols = (C_HQ, C_HI, C_HF_FW + d_idx) if hg else (C_GQ, C_GK, C_GV)

    def body(*refs):
        if hg:
            a_ref, b_ref, c_ref, lb_ref, st_ref, do_ref, da_ref, db_ref, dc_ref, dlb_ref, dstate = refs
        else:
            (a_ref, b_ref, c_ref, lr_ref, wgk_ref, bgk_ref, st_ref, do_ref, da_ref, db_ref, dc_ref, dlr_ref,
             dwgk_ref, dbias_ref, dstate) = refs
            lr = lr_ref[...]
            logits = _dot(lr, wgk_ref[...]) + bgk_ref[...]
            g_all = _log_sigmoid(logits) * (1.0 / GATE_NORM)

        @pl.when(pl.program_id(0) == 0)
        def _():
            dstate[...] = jnp.zeros_like(dstate)
            if hg:
                dlb_ref[...] = jnp.zeros_like(dlb_ref)
            else:
                dwgk_ref[...] = jnp.zeros_like(dwgk_ref)
                dbias_ref[...] = jnp.zeros_like(dbias_ref)

        dg_parts = []
        for h in range(NH):
            sl = slice(h * HD, (h + 1) * HD)
            if hg:
                hq, hf = a_ref[:, sl], c_ref[:, sl]
                q, k, g, f, sg, lb = _hg_inputs(hq, hf, lb_ref[...], d_idx, sl)
                v = b_ref[:, sl]
            else:
                q, k, v, g = a_ref[:, sl] * (HD ** -0.5), b_ref[:, sl], c_ref[:, sl], g_all[:, sl]
            dq, dk, dv, dg, dst0 = _chunk_bwd(q, k, v, g, st_ref[0, h], do_ref[:, sl], dstate[h], fw)
            dstate[h] = dst0
            if hg:
                da_ref[:, sl] = dq * _dsilu(hq)
                db_ref[:, sl] = dv
                df = dg / f - dk
                dc_ref[:, sl] = df * (1.0 - lb) * sg * (1.0 - sg)
                dlb_ref[0:1, sl] += _colsum(df * (1.0 - sg))
            else:
                da_ref[:, sl] = dq * (HD ** -0.5)
                db_ref[:, sl] = dk
                dc_ref[:, sl] = dv
                dg_parts.append(dg)
        if not hg:
            dlogits = jnp.concatenate(dg_parts, axis=1) * (1.0 / GATE_NORM) * (1.0 - _sigmoid(logits))
            dlr_ref[...] = _dot_nt(dlogits, wgk_ref[...])
            dwgk_ref[...] += _dot_tn(lr, dlogits)
            dbias_ref[0:1, :] += _colsum(dlogits)

    def chunk_of(j):
        return _chunk_index(n_chunks - 1 - j, n_ctx_chunks, n_chunks, fw)

    def cmap(blk, width=HW):
        return pl.BlockSpec((CHUNK, width), lambda j: (chunk_of(j), blk))

    fixed = lambda j: (0, 0)
    st_spec = pl.BlockSpec((1, NH, HD, HD), lambda j: (chunk_of(j), 0, 0, 0))
    big = jax.ShapeDtypeStruct((rows, HW), F32)
    if hg:
        in_specs = [cmap(cols[0]), cmap(cols[1]), cmap(cols[2]), pl.BlockSpec((4, HW), fixed), st_spec, cmap(0)]
        args = (p, p, p, side, states, d_o)
        out_shape = [big, big, big, jax.ShapeDtypeStruct((8, HW), F32)]
        out_specs = [cmap(0), cmap(0), cmap(0), pl.BlockSpec((8, HW), fixed)]
    else:
        in_specs = [cmap(cols[0]), cmap(cols[1]), cmap(cols[2]), cmap(OFF_LR // 128, 128),
                    pl.BlockSpec((128, HW), fixed), pl.BlockSpec((1, HW), fixed), st_spec, cmap(0)]
        args = (p, p, p, p, side[0], side[1], states, d_o)
        out_shape = [big, big, big, jax.ShapeDtypeStruct((rows, 128), F32), jax.ShapeDtypeStruct((128, HW), F32),
                     jax.ShapeDtypeStruct((8, HW), F32)]
        out_specs = [cmap(0), cmap(0), cmap(0), cmap(0, 128), pl.BlockSpec((128, HW), fixed),
                     pl.BlockSpec((8, HW), fixed)]
    return pl.pallas_call(
        body, name=name, grid=(n_chunks,), out_shape=out_shape, in_specs=in_specs, out_specs=out_specs,
        scratch_shapes=[pltpu.VMEM((NH, HD, HD), F32)],
        compiler_params=_cparams(dimension_semantics=("arbitrary",)),
    )(*args)


SMALL_ROWS = 56
ROWS_MOD_X = (0, 1, 8, 16, 17, 18)
ROWS_MOD_C = (2, 3)
ROW_PRE1, ROW_POST1, ROW_ONORM, ROW_PRE2, ROW_POST2, ROW_LB, ROW_BGK, ROW_WGK = 4, 9, 10, 19, 20, 24, 32, 40
ROW_LOSS = 21


def _reduce_small(gathered, lb_full, name):
    _, _, d = gathered.shape

    def body(g_ref, lb_ref, sum_ref, dmod_ref, dbmod_ref, dlb_ref):
        total = g_ref[0]
        for b in range(1, N_DEV):
            total = total + g_ref[b]
        sum_ref[...] = total
        dmod_ref[...] = jnp.zeros_like(dmod_ref)
        for m in range(N_MOD):
            col = slice(m * d, (m + 1) * d)
            acc = jnp.zeros((1, d), F32)
            for b in range(N_DEV):
                row = g_ref[b, ROWS_MOD_X[m]:ROWS_MOD_X[m] + 1, :]
                dmod_ref[b:b + 1, col] = row
                acc = acc + row
            if m < 2:
                ctx_row = total[ROWS_MOD_C[m]:ROWS_MOD_C[m] + 1, :]
                dmod_ref[8:9, col] = ctx_row
                acc = acc + ctx_row
            dbmod_ref[:, col] = acc
        lbv = lb_ref[...]
        for dd in range(2):
            lb = _sigmoid(lbv[dd:dd + 1, :] - lbv[2 + dd:3 + dd, :])
            gl = total[ROW_LB:ROW_LB + 1, dd * HW:(dd + 1) * HW] * lb * (1.0 - lb)
            dlb_ref[dd:dd + 1, :] = gl
            dlb_ref[2 + dd:3 + dd, :] = -gl

    return pl.pallas_call(
        body, name=name,
        out_shape=[jax.ShapeDtypeStruct((SMALL_ROWS, d), F32), jax.ShapeDtypeStruct((16, N_MOD * d), F32),
                   jax.ShapeDtypeStruct((1, N_MOD * d), F32), jax.ShapeDtypeStruct((4, HW), F32)],
        in_specs=[VMEM_SPEC] * 2, out_specs=[VMEM_SPEC] * 4, compiler_params=_cparams(),
    )(gathered, lb_full)


def _c_ctx_grad(gathered, c_ctx_row, name):
    def body(g_ref, c_ref, o_ref):
        acc = g_ref[0, 0:1, :]
        for chip in range(1, N_CHIP):
            acc = acc + g_ref[2 * chip, 0:1, :]
        o_ref[...] = acc * _dsilu(c_ref[...])

    return pl.pallas_call(
        body, name=name, out_shape=jax.ShapeDtypeStruct(c_ctx_row.shape, F32),
        in_specs=[VMEM_SPEC] * 2, out_specs=VMEM_SPEC, compiler_params=_cparams(),
    )(gathered, c_ctx_row)


def _relayout_w_in(w):
    pad = jnp.zeros((w.shape[0], 128 - 2 * RANK), w.dtype)
    return jnp.concatenate([w[:, :9 * HW], w[:, 9 * HW + 2 * RANK:], w[:, 9 * HW:9 * HW + 2 * RANK], pad], axis=1)


def _relayout_w_in_rows(wt):
    pad = jnp.zeros((128 - 2 * RANK, wt.shape[1]), wt.dtype)
    return jnp.concatenate([wt[:9 * HW], wt[9 * HW + 2 * RANK:], wt[9 * HW:9 * HW + 2 * RANK], pad], axis=0)


def _w_in_grad_rows(g_main, g_lr):
    return jnp.concatenate([g_main[:9 * HW], g_lr[:2 * RANK], g_main[9 * HW:]], axis=0)


def _w_in_grad_blocks(g_main, g_lr, n_blocks):
    lr0 = 9 * HW
    n = (g_main.shape[1] + 2 * RANK) // n_blocks

    def cols(lo, hi):
        out = []
        if lo < lr0:
            out.append(g_main[:, lo:min(hi, lr0)])
        if hi > lr0 and lo < lr0 + 2 * RANK:
            out.append(g_lr[:, max(lo, lr0) - lr0:min(hi, lr0 + 2 * RANK) - lr0])
        if hi > lr0 + 2 * RANK:
            out.append(g_main[:, max(lo, lr0 + 2 * RANK) - 2 * RANK:hi - 2 * RANK])
        return out

    return jnp.stack([jnp.concatenate(cols(j * n, (j + 1) * n), axis=1) for j in range(n_blocks)])


def _blocked(full, n_blocks):
    k, n = full.shape
    return full.reshape(k, n_blocks, n // n_blocks).transpose(1, 0, 2)


def _unblocked(blocks):
    nb, k, n = blocks.shape
    return blocks.transpose(1, 0, 2).reshape(k, nb * n)


def _sample_front(x0, ctx0, modc, modx, norm_pre1, lb_full, gla_side, w_in_r):
    ctx_len = ctx0.shape[0]
    n_ctx_tiles = ctx_len // TM
    n_ctx_chunks = ctx_len // CHUNK
    h1, p = _in_projection(ctx0, x0, modc, modx, norm_pre1, w_in_r, n_ctx_tiles, "in_projection")
    o_hg_fw, st_hg_fw, o_hg_bw, st_hg_bw = _scan_fwd_both(p, lb_full, n_ctx_chunks, "hg", "scan_hg")
    o_gla_fw, st_gla_fw, o_gla_bw, st_gla_bw = _scan_fwd_both(p, gla_side, n_ctx_chunks, "gla", "scan_gla")
    return dict(h1=h1, p=p, o_list=[o_hg_fw, o_hg_bw, o_gla_fw, o_gla_bw],
                states=[st_hg_fw, st_hg_bw, st_gla_fw, st_gla_bw])


def _sample_back(reduce, front, x0, ctx0, target0, modc, modx, norm_pre1, norms, onorms, lb_full, gla_side, w_in_r,
                 wbh, wbg, wout, wg, wu, wd):
    seq, d = x0.shape
    ctx_len = ctx0.shape[0]
    n_ctx_tiles = ctx_len // TM
    n_tiles = (ctx_len + seq) // TM
    n_ctx_chunks = ctx_len // CHUNK
    h1, p, o_list = front["h1"], front["p"], front["o_list"]
    st_hg_fw, st_hg_bw, st_gla_fw, st_gla_bw = front["states"]
    z2, y1, merged, og_hg, og_gla = _mixer_tail_fwd(x0, p, o_list, modx, norms, onorms, wbh, wbg, wout, n_ctx_tiles,
                                                    "mixer_tail")
    loss_part, dz2, h2, a_act, du, dv, dy2, stat_ffn = _ffn_fwd_bwd(z2, modx, norms, wg, wu, wd, target0, "ffn")
    dff = wg.shape[0]
    tok = reduce("ffn", [_weight_grad(du, h2, "grad_w_ff_gate", tk=dff // 2, tn=d),
                         _weight_grad(dv, h2, "grad_w_ff_up", tk=dff // 2, tn=d),
                         _weight_grad(a_act, dy2, "grad_w_ff_down", tk=dff // 2)])

    (d_ohg, d_ogla, d_hgate, d_ggate, d_ghg, d_ggla, dy1, db_hg, db_gla, stat_mix) = _mixer_tail_bwd(
        x0, p, o_list, dz2, y1, modx + tok, norms, onorms, wbh, wbg, wout, n_ctx_tiles, n_tiles, "mixer_tail_bwd")
    tok = reduce("mix", [_weight_grad(og_hg, db_hg, "grad_w_br_hg"), _weight_grad(og_gla, db_gla, "grad_w_br_gla"),
                         _weight_grad(merged, dy1, "grad_w_out")])
    tok = tok + reduce("push_ffn", [dy1])
    gla_b = [(wgk, bias + tok) for wgk, bias in gla_side]
    (dgq_f, dgk_f, dgv_f, dlr_f, dwgk_f, dbgk_f, dgq_b, dgk_b, dgv_b, dlr_b, dwgk_b, dbgk_b) = _scan_bwd_both(
        p, gla_b, (st_gla_fw, st_gla_bw), d_ogla, n_ctx_chunks, "gla", "scan_gla_bwd")
    lb_b = lb_full + reduce("push_mix", [dbgk_f])
    (dhq_f, dhi_f, dhf_f, dlb_f, dhq_b, dhi_b, dhf_b, dlb_b) = _scan_bwd_both(
        p, lb_b, (st_hg_fw, st_hg_bw), d_ohg, n_ctx_chunks, "hg", "scan_hg_bwd")
    pieces = [dhq_f, dhq_b, dhi_f, dhi_b, dhf_f, dhf_b, d_hgate, dgq_f, dgq_b, dgk_f, dgk_b, dgv_f, dgv_b, d_ggate,
              d_ghg, d_ggla, dlr_f, dlr_b]
    dp, grad_x, stat_in = _in_projection_bwd(ctx0, x0, dz2, modc, modx, norm_pre1, w_in_r, pieces, n_ctx_tiles,
                                             "in_projection_bwd")

    tok = reduce("in", [_w_in_grad(dp, h1, w_in_r.shape[0], "grad_w_in")])
    reduce("small", dict(stat_in=stat_in + tok, stat_mix=stat_mix, stat_ffn=stat_ffn, dlb=(dlb_f, dlb_b),
                         dwgk=(dwgk_f, dwgk_b), dbgk=(dbgk_f, dbgk_b)))
    reduce("push_in", [])
    return dict(
        loss_part=loss_part, grad_x=grad_x, stat_in=stat_in, stat_mix=stat_mix, stat_ffn=stat_ffn,
        dlb=(dlb_f, dlb_b), dwgk=(dwgk_f, dwgk_b), dbgk=(dbgk_f, dbgk_b))


def kernel(x, c, ctx, c_ctx, w_mod, b_mod, norm_pre1, norm_post1, norm_pre2, norm_post2, w_in, hg_lb, hg_onorm, gla_w_gk, gla_b_gk, gla_onorm, w_br_hg, w_br_gla, w_out, w_ff_gate, w_ff_up, w_ff_down, loss_target, m_c_ctx, m_w_mod, m_b_mod, m_norm_pre1, m_norm_post1, m_norm_pre2, m_norm_post2, m_w_in, m_hg_lb, m_hg_onorm, m_gla_w_gk, m_gla_b_gk, m_gla_onorm, m_w_br_hg, m_w_br_gla, m_w_out, m_w_ff_gate, m_w_ff_up, m_w_ff_down, v_c_ctx, v_w_mod, v_b_mod, v_norm_pre1, v_norm_post1, v_norm_pre2, v_norm_post2, v_w_in, v_hg_lb, v_hg_onorm, v_gla_w_gk, v_gla_b_gk, v_gla_onorm, v_w_br_hg, v_w_br_gla, v_w_out, v_w_ff_gate, v_w_ff_up, v_w_ff_down):
    seq, d = x.shape[1], x.shape[2]
    ctx_len = ctx.shape[1]
    assert seq % TM == 0 and ctx_len % TM == 0 and d == 2 * HW
    ax, ay, ac = lax.axis_index("x"), lax.axis_index("y"), lax.axis_index("c")
    chip = 2 * ax + ay
    dev = 2 * chip + ac
    c_arr = jnp.reshape(ac, (1,)).astype(jnp.int32)
    chip_arr = jnp.reshape(chip, (1,)).astype(jnp.int32)
    transposed = ("w_in", "w_ff_gate", "w_ff_up")
    view = lambda a, nm: a[0].T if nm in transposed else a[0]

    sems_in, lands_in, token_in0 = _blocks_start([_cast_into_blocks(chip_arr, view(w_in, "w_in"), "cast_w_in")],
                                                 "gather_w_in_start")

    nc = d // 128
    pad8 = lambda a: jnp.pad(a, ((0, -a.shape[0] % 8), (0, 0)))
    small1 = jnp.concatenate([c.reshape(nc, 128) + token_in0[0, 0], pad8(hg_lb.reshape(4, 128)),
                              gla_w_gk.reshape(2 * RANK, 128), pad8(gla_b_gk.reshape(2, 128))], axis=0)
    got1 = _allgather8(small1, "gather_small_params")
    c_all = got1[:, :nc, :].reshape(N_DEV, d)
    per_chip = got1[0::2]
    lb_full = per_chip[:, nc:nc + 4, :].transpose(1, 0, 2).reshape(4, HW)
    wgk_full = per_chip[:, nc + 8:nc + 8 + 2 * RANK, :].transpose(1, 0, 2).reshape(2, RANK, HW)
    bgk_full = per_chip[:, nc + 8 + 2 * RANK:nc + 10 + 2 * RANK, :].transpose(1, 0, 2).reshape(2, HW)
    wgk_pad = [jnp.zeros((128, HW), F32).at[dd * RANK:(dd + 1) * RANK].set(wgk_full[dd]) for dd in range(2)]
    bgk = [bgk_full[dd:dd + 1] for dd in range(2)]

    n_mod_cols = w_mod.shape[2]
    cond = jnp.concatenate([c_all, pad8(c_ctx.reshape(1, d))], axis=0)
    b_cols = lax.dynamic_slice(b_mod, (0, chip * n_mod_cols), (1, n_mod_cols))
    mod_part = _mod_forward(cond, w_mod[0], b_cols, "mod_forward")
    mod_got = _allgather8(mod_part, "gather_mod")
    mod_all = mod_got[0::2].transpose(1, 0, 2).reshape(16, N_CHIP * n_mod_cols)
    modx = pad8(lax.dynamic_slice(mod_all, (dev, 0), (1, N_MOD * d)).reshape(N_MOD, d))
    modc = pad8(mod_all[8].reshape(N_MOD, d))

    blocks = [_cast_into_blocks(chip_arr, view(w_, nm), "cast_" + nm) for w_, nm in (
        (w_br_hg, "w_br_hg"), (w_br_gla, "w_br_gla"), (w_out, "w_out"), (w_ff_gate, "w_ff_gate"),
        (w_ff_up, "w_ff_up"), (w_ff_down, "w_ff_down"))]
    lands_in = _blocks_wait(sems_in, lands_in, [mod_got], "gather_w_in_wait")
    gathered_in = _blocks_finish(lands_in, "gather_w_in_finish")
    sems, lands, token = _blocks_start(blocks, "gather_rest_start", after=[gathered_in[0]])
    w_in_r = gathered_in[0].reshape(-1, d)

    norms = jnp.concatenate([norm_pre1, norm_post1, norm_pre2, norm_post2, jnp.zeros((4, d), F32)], axis=0)
    onorms = jnp.zeros((8, d), F32).at[0, :HD].set(hg_onorm[0]).at[1, :HD].set(gla_onorm[0])
    gla_side = [(wgk_pad[dd], bgk[dd]) for dd in range(2)]
    modx = modx + token[0, 0]
    front = _sample_front(x[0], ctx[0], modc, modx, norm_pre1, lb_full, gla_side, w_in_r)
    lands = _blocks_wait(sems, lands, front["o_list"], "gather_rest_wait")
    gathered = _blocks_finish(lands, "gather_rest_finish")
    wbh, wbg = _unblocked(gathered[0]), _unblocked(gathered[1])
    wout = gathered[2].reshape(d, d)
    wg, wu, wd = (gathered[i].reshape(-1, d) for i in (3, 4, 5))
    dff = wg.shape[0]
    groups = {"ffn": ["w_ff_gate", "w_ff_up", "w_ff_down"], "mix": ["w_br_hg", "w_br_gla", "w_out"], "in": ["w_in"]}
    row_sharded = {"w_out": d // N_CHIP, "w_ff_down": dff // N_CHIP, "w_ff_gate": dff // N_CHIP,
                   "w_ff_up": dff // N_CHIP, "w_in": w_in.shape[2]}
    in_flight, to_sibling = {}, {}

    small = {}

    def reduce_small(stats):
        small2 = jnp.concatenate([
            stats["stat_in"], stats["stat_mix"], stats["stat_ffn"],
            jnp.concatenate(stats["dlb"], axis=1), jnp.concatenate(stats["dbgk"], axis=1),
            jnp.concatenate([stats["dwgk"][0][0:RANK], stats["dwgk"][1][RANK:2 * RANK]], axis=1)], axis=0)
        assert small2.shape[0] == SMALL_ROWS
        got2 = _allgather8(small2, "gather_small_grads")
        total, dmod_all, g_b_mod, g_lb_full = _reduce_small(got2, lb_full, "reduce_small")
        dmod_cols = lax.dynamic_slice(dmod_all, (0, chip * n_mod_cols), (16, n_mod_cols))
        g_w_mod, cctx_part = _mod_backward(cond, w_mod[0], dmod_cols, "mod_backward")
        got3 = _allgather8(cctx_part, "gather_c_ctx_grad")
        g_c_ctx = _c_ctx_grad(got3, c_ctx.reshape(1, d), "c_ctx_grad")
        small.update(total=total, g_b_mod=g_b_mod, g_lb_full=g_lb_full, g_w_mod=g_w_mod, g_c_ctx=g_c_ctx)

    def reduce(group, grads):
        if group == "small":
            return reduce_small(grads)
        if group.startswith("push_"):
            return push(group[5:], grads)
        nms = groups[group]
        full = [g.reshape(N_CHIP, row_sharded[nm], d) if nm in row_sharded else _blocked(g, N_CHIP)
                for g, nm in zip(grads, nms)]
        sems_, full, lands_, token_ = _send_half_start(full, "grads_to_sibling_start_" + group)
        to_sibling[group] = (sems_, full, lands_)
        return token_[0, 0]

    def push(group, after):
        nms = groups[group]
        sems_, full, lands_ = to_sibling[group]
        if group == "in":
            after = list(after) + [small["g_c_ctx"], small["total"]]
        full, from_sibling = _send_half_wait(sems_, full, lands_, after, "grads_to_sibling_wait_" + group)
        pairs = [_pair_sum(c_arr, f, r_, "pair_sum_" + nm) for f, r_, nm in zip(full, from_sibling, nms)]
        after = [small["g_c_ctx"], small["total"]] if group == "in" else []
        sems_, pairs, lands_, token_ = _scatter_start(pairs, "grads_to_owner_start_" + group, after)
        in_flight[group] = (sems_, pairs, lands_, token_)
        return token_[0, 0]

    r = _sample_back(reduce, front, x[0], ctx[0], loss_target[0], modc, modx, norm_pre1, norms, onorms, lb_full,
                     gla_side, w_in_r, wbh, wbg, wout, wg, wu, wd)
    loss_part, grad_x, stat_in, stat_mix, stat_ffn = (r[k] for k in ("loss_part", "grad_x", "stat_in", "stat_mix",
                                                                     "stat_ffn"))
    (dlb_f, dlb_b), (dwgk_f, dwgk_b), (dbgk_f, dbgk_b) = r["dlb"], r["dwgk"], r["dbgk"]

    weights = dict(w_in=(w_in, m_w_in, v_w_in), w_br_hg=(w_br_hg, m_w_br_hg, v_w_br_hg),
                   w_br_gla=(w_br_gla, m_w_br_gla, v_w_br_gla), w_out=(w_out, m_w_out, v_w_out),
                   w_ff_gate=(w_ff_gate, m_w_ff_gate, v_w_ff_gate), w_ff_up=(w_ff_up, m_w_ff_up, v_w_ff_up),
                   w_ff_down=(w_ff_down, m_w_ff_down, v_w_ff_down))
    names = ["w_in", "w_br_hg", "w_br_gla", "w_out", "w_ff_gate", "w_ff_up", "w_ff_down"]
    big = {}

    def finish(group, after):
        sems_, pairs, lands_, _ = in_flight[group]
        pairs, lands_ = _scatter_wait(sems_, pairs, lands_, after, "grads_to_owner_wait_" + group)
        own_half = [_sum_owner(chip_arr, pr, g, "chip_sum_" + nm) for pr, g, nm in zip(pairs, lands_, groups[group])]
        other_half = _swap_with_sibling(own_half, "halves_to_sibling_" + group)
        for nm, own, oth in zip(groups[group], own_half, other_half):
            w_, m_, v_ = (view(a, nm) for a in weights[nm])
            res = _adamw_halves(c_arr, own, oth, w_, m_, v_, "adamw_" + nm)
            big[nm] = [r_.T[None] if nm in transposed else r_[None] for r_ in res]
        return big[groups[group][-1]][1]

    token_in = in_flight["in"][3]
    done_ffn = finish("ffn", [token_in])
    done_mix = finish("mix", [done_ffn])

    total, g_b_mod, g_lb_full, g_w_mod, g_c_ctx = (small[k] for k in ("total", "g_b_mod", "g_lb_full", "g_w_mod",
                                                                      "g_c_ctx"))
    g_pre1, g_post1, g_pre2, g_post2 = (total[r_:r_ + 1] for r_ in (ROW_PRE1, ROW_POST1, ROW_PRE2, ROW_POST2))
    g_hg_on, g_gla_on = total[ROW_ONORM:ROW_ONORM + 1, 0:HD], total[ROW_ONORM:ROW_ONORM + 1, HD:2 * HD]
    n_lb = hg_lb.shape[2]
    g_hg_lb = lax.dynamic_slice(g_lb_full, (0, chip * n_lb), (4, n_lb))
    g_bgk = lax.dynamic_slice(total[ROW_BGK:ROW_BGK + 1].reshape(2, HW), (0, chip * n_lb), (2, n_lb))
    g_wgk_full = total[ROW_WGK:ROW_WGK + RANK].reshape(RANK, 2, HW).transpose(1, 0, 2).reshape(2 * RANK, HW)
    g_wgk = lax.dynamic_slice(g_wgk_full, (0, chip * n_lb), (2 * RANK, n_lb))

    small_items = [
        (g_c_ctx, c_ctx.reshape(1, d), m_c_ctx.reshape(1, d), v_c_ctx.reshape(1, d)),
        (g_b_mod, b_mod, m_b_mod, v_b_mod),
        (g_pre1, norm_pre1, m_norm_pre1, v_norm_pre1),
        (g_post1, norm_post1, m_norm_post1, v_norm_post1),
        (g_pre2, norm_pre2, m_norm_pre2, v_norm_pre2),
        (g_post2, norm_post2, m_norm_post2, v_norm_post2),
        (g_hg_lb, hg_lb.reshape(4, n_lb), m_hg_lb.reshape(4, n_lb), v_hg_lb.reshape(4, n_lb)),
        (g_hg_on, hg_onorm, m_hg_onorm, v_hg_onorm),
        (g_wgk, gla_w_gk.reshape(2 * RANK, n_lb), m_gla_w_gk.reshape(2 * RANK, n_lb), v_gla_w_gk.reshape(2 * RANK, n_lb)),
        (g_bgk, gla_b_gk.reshape(2, n_lb), m_gla_b_gk.reshape(2, n_lb), v_gla_b_gk.reshape(2, n_lb)),
        (g_gla_on, gla_onorm, m_gla_onorm, v_gla_onorm),
    ]
    small_res = _adamw_whole(small_items, "adamw_small")
    mod_res = _adamw_tiled(g_w_mod, w_mod[0], m_w_mod[0], v_w_mod[0], "adamw_w_mod")
    finish("in", [done_mix, mod_res[0], small_res[0][0]])

    loss = total[ROW_LOSS, 0]

    shapes = dict(c_ctx=c_ctx.shape, b_mod=b_mod.shape, norm_pre1=norm_pre1.shape, norm_post1=norm_post1.shape,
                  norm_pre2=norm_pre2.shape, norm_post2=norm_post2.shape, hg_lb=hg_lb.shape, hg_onorm=hg_onorm.shape,
                  gla_w_gk=gla_w_gk.shape, gla_b_gk=gla_b_gk.shape, gla_onorm=gla_onorm.shape)
    small_names = ["c_ctx", "b_mod", "norm_pre1", "norm_post1", "norm_pre2", "norm_post2", "hg_lb", "hg_onorm",
                   "gla_w_gk", "gla_b_gk", "gla_onorm"]
    grads, deltas, new_m, new_v = {}, {}, {}, {}
    for nm, item, res in zip(small_names, small_items, small_res):
        grads[nm] = item[0].reshape(shapes[nm])
        deltas[nm], new_m[nm], new_v[nm] = (r.reshape(shapes[nm]) for r in res)
    grads["w_mod"] = g_w_mod[None]
    deltas["w_mod"], new_m["w_mod"], new_v["w_mod"] = (r[None] for r in mod_res)
    for nm in names:
        grads[nm], deltas[nm], new_m[nm], new_v[nm] = big[nm]
    order = ["c_ctx", "w_mod", "b_mod", "norm_pre1", "norm_post1", "norm_pre2", "norm_post2", "w_in", "hg_lb",
             "hg_onorm", "gla_w_gk", "gla_b_gk", "gla_onorm", "w_br_hg", "w_br_gla", "w_out", "w_ff_gate", "w_ff_up",
             "w_ff_down"]
    return (loss, grad_x[None], *[grads[n] for n in order], *[deltas[n] for n in order],
            *[new_m[n] for n in order], *[new_v[n] for n in order])


def _weight_grad_cols(xs, dy, n_cols, name, tn=512):
    rows = dy.shape[0]
    k = tk = xs.shape[1]

    return pl.pallas_call(
        functools.partial(_transposed_lhs_matmul), name=name, grid=(k // tk, n_cols // tn),
        out_shape=jax.ShapeDtypeStruct((k, n_cols), F32),
        in_specs=[pl.BlockSpec((rows, tk), lambda i, j: (0, i)), pl.BlockSpec((rows, tn), lambda i, j: (0, j))],
        out_specs=pl.BlockSpec((tk, tn), lambda i, j: (i, j)),
        scratch_shapes=[pltpu.VMEM((tk, rows), BF16)],
        compiler_params=_cparams(dimension_semantics=("parallel", "arbitrary")),
    )(xs, dy)
```

```python
import functools

import jax
import jax.numpy as jnp
from jax import lax
from jax.experimental import pallas as pl
from jax.experimental.pallas import tpu as pltpu

F32 = jnp.float32
BF16 = jnp.bfloat16
HIGHEST = lax.Precision.HIGHEST
MESH = pl.DeviceIdType.MESH

EPS = 1e-6
CHUNK = 64
SUB = 16
NSUB = CHUNK // SUB
NH = 4
HD = 128
HW = NH * HD
RANK = 16
GATE_NORM = 16.0
N_MOD = 6
TM = 256
TM_FFN = 128
N_DEV = 8
N_CHIP = 4
VMEM_LIMIT = 56 * 1024 * 1024

ADAM_LR = 0.001
ADAM_B1 = 0.9
ADAM_B2 = 0.999
ADAM_EPS = 1e-08
ADAM_WD = 0.01
ADAM_STEP = 10

VMEM_SPEC = pl.BlockSpec(memory_space=pltpu.VMEM)
ANY_SPEC = pl.BlockSpec(memory_space=pl.ANY)
HBM_SPEC = pl.BlockSpec(memory_space=pltpu.HBM)
SEM_SPEC = pl.BlockSpec(memory_space=pltpu.SEMAPHORE)
EFFECT = pltpu.SideEffectType.DATAFLOW_SIDE_EFFECTING


def _cparams(**kw):
    return pltpu.CompilerParams(vmem_limit_bytes=VMEM_LIMIT, **kw)


def _dot(a, b):
    return jnp.dot(a.astype(BF16), b.astype(BF16), preferred_element_type=F32)


def _dot_nt(a, b):
    return lax.dot_general(a.astype(BF16), b.astype(BF16), (((1,), (1,)), ((), ())), preferred_element_type=F32)


def _dot_tn(a, b):
    return lax.dot_general(a.astype(BF16), b.astype(BF16), (((0,), (0,)), ((), ())), preferred_element_type=F32)


def _sigmoid(x):
    return 1.0 / (1.0 + jnp.exp(-x))


def _silu(x):
    return x * _sigmoid(x)


def _dsilu(x):
    s = _sigmoid(x)
    return s * (1.0 + x * (1.0 - s))


def _log_sigmoid(x):
    return jnp.minimum(x, 0.0) - jnp.log(1.0 + jnp.exp(-jnp.abs(x)))


def _colsum(a):
    return jnp.sum(a, axis=0, keepdims=True)


def _rms(a):
    r = lax.rsqrt(jnp.mean(a * a, axis=-1, keepdims=True) + EPS)
    return a * r, r


def _rms_bwd(dn, n, r):
    return r * (dn - n * jnp.mean(dn * n, axis=-1, keepdims=True))


def _place():
    x, y, c = lax.axis_index("x"), lax.axis_index("y"), lax.axis_index("c")
    chips = [(1 - x, y), (x, 1 - y), (1 - x, 1 - y)]
    return x, y, c, chips


def _allgather8(v, name, after=()):
    rows, cols = v.shape
    n_after = len(after)

    def body(x_ref, *rest):
        out_ref, send_sems, recv_sems, local_sem = rest[n_after:]
        x, y, c, chips = _place()
        me, sibling = (x, y, c), (x, y, 1 - c)

        def blk(px, py, pc):
            return out_ref.at[4 * px + 2 * py + pc]

        def copy(k, block, to, src=None):
            return pltpu.make_async_remote_copy(
                src_ref=blk(*block) if src is None else src, dst_ref=blk(*block),
                send_sem=send_sems.at[k], recv_sem=recv_sems.at[k], device_id=to, device_id_type=MESH)

        mine = pltpu.make_async_copy(x_ref, blk(*me), local_sem)
        mine.start()
        first = [copy(0, me, sibling, src=x_ref)]
        first += [copy(1 + j, me, (*chip, c), src=x_ref) for j, chip in enumerate(chips)]
        for cp in first:
            cp.start()
        passed = [copy(4 + j, (*chip, c), sibling) for j, chip in enumerate(chips)]
        for j, chip in enumerate(chips):
            copy(1 + j, (*chip, c), me).wait_recv()
            passed[j].start()
        copy(0, sibling, me).wait_recv()
        for j, chip in enumerate(chips):
            copy(4 + j, (*chip, 1 - c), me).wait_recv()
        for cp in first + passed:
            cp.wait_send()
        mine.wait()

    return pl.pallas_call(
        body, name=name,
        out_shape=jax.ShapeDtypeStruct((N_DEV, rows, cols), v.dtype),
        in_specs=[VMEM_SPEC] + [ANY_SPEC] * n_after, out_specs=VMEM_SPEC,
        scratch_shapes=[pltpu.SemaphoreType.DMA((7,)), pltpu.SemaphoreType.DMA((7,)), pltpu.SemaphoreType.DMA],
    )(v, *after)


def _cast_into_blocks(chip_arr, w, name):
    rows, cols = w.shape
    tr = _row_tile(rows, 16, 256)

    def body(chip_ref, w_ref, o_ref):
        o_ref[0] = w_ref[...].astype(BF16)

    return pl.pallas_call(
        body, name=name,
        grid_spec=pltpu.PrefetchScalarGridSpec(
            num_scalar_prefetch=1, grid=(rows // tr,),
            in_specs=[pl.BlockSpec((tr, cols), lambda i, chip_ref: (i, 0))],
            out_specs=pl.BlockSpec((1, tr, cols), lambda i, chip_ref: (chip_ref[0], i, 0))),
        out_shape=jax.ShapeDtypeStruct((N_CHIP, rows, cols), BF16),
        compiler_params=_cparams(dimension_semantics=("parallel",)),
    )(chip_arr, w)


def _halved_by_rows(shape):
    return (shape[1] // 2) % 16 == 0


def _half_of(ref, pc, block=None):
    lead = slice(None) if block is None else block
    if _halved_by_rows(ref.shape):
        h = ref.shape[1] // 2
        return ref.at[lead, pl.ds(pl.multiple_of(pc * h, 16), h), :]
    h = ref.shape[2] // 2
    return ref.at[lead, :, pl.ds(pl.multiple_of(pc * h, 128), h)]


def _half_shape(shape):
    return (shape[0], shape[1] // 2, shape[2]) if _halved_by_rows(shape) else (shape[0], shape[1], shape[2] // 2)


def _half_rows(ref, chip_id, pc):
    return _half_of(ref, pc, chip_id)


def _gather_blocks(lands, name, after=()):
    n = len(lands)
    n_in = n + len(after)

    def body(*refs):
        outs = refs[n_in:n_in + n]
        send_sems, recv_sems = refs[n_in + n:]
        x, y, c, chips = _place()
        me_chip = 2 * x + y
        sibling = (x, y, 1 - c)

        def copy(k, j, chip_id, pc, to):
            return pltpu.make_async_remote_copy(
                src_ref=_half_rows(outs[k], chip_id, pc), dst_ref=_half_rows(outs[k], chip_id, pc),
                send_sem=send_sems.at[k, j], recv_sem=recv_sems.at[k, j], device_id=to, device_id_type=MESH)

        started = []
        for k in range(n):
            for j, chip in enumerate(chips):
                cp = copy(k, j, me_chip, c, (*chip, c))
                cp.start()
                started.append(cp)
        for k in range(n):
            for j, (px, py) in enumerate(chips):
                copy(k, j, 2 * px + py, c, sibling).wait_recv()
                cp = copy(k, 3 + j, 2 * px + py, c, sibling)
                cp.start()
                started.append(cp)
        for k in range(n):
            for j, (px, py) in enumerate(chips):
                copy(k, 3 + j, 2 * px + py, 1 - c, sibling).wait_recv()
        for cp in started:
            cp.wait_send()

    return pl.pallas_call(
        body, name=name,
        out_shape=[jax.ShapeDtypeStruct(l.shape, l.dtype) for l in lands],
        in_specs=[ANY_SPEC] * n_in, out_specs=[ANY_SPEC] * n,
        input_output_aliases={i: i for i in range(n)},
        scratch_shapes=[pltpu.SemaphoreType.DMA((n, 6)), pltpu.SemaphoreType.DMA((n, 6))],
    )(*lands, *after)


def _hbm(a):
    return pltpu.with_memory_space_constraint(a, pltpu.HBM)


def _blocks_start(lands, name, after=()):
    n = len(lands)
    n_sem = 3 * n
    first = n + len(after)

    def body(*refs):
        lnd = refs[:n]
        send_sems, recv_sems = refs[first:first + n_sem], refs[first + n_sem:first + 2 * n_sem]
        token = refs[-1]
        x, y, c, chips = _place()
        me_chip = 2 * x + y
        for k in range(n):
            for j, chip in enumerate(chips):
                pltpu.make_async_remote_copy(
                    src_ref=_half_rows(lnd[k], me_chip, c), dst_ref=_half_rows(lnd[k], me_chip, c),
                    send_sem=send_sems[3 * k + j], recv_sem=recv_sems[3 * k + j],
                    device_id=(*chip, c), device_id_type=MESH).start()
        token[...] = jnp.zeros_like(token)

    out = pl.pallas_call(
        body, name=name,
        out_shape=(*[pltpu.SemaphoreType.DMA(())] * (2 * n_sem),
                   *[pltpu.HBM(l.shape, l.dtype) for l in lands],
                   jax.ShapeDtypeStruct((8, 128), F32)),
        in_specs=[HBM_SPEC] * n + [ANY_SPEC] * len(after),
        out_specs=(*[SEM_SPEC] * (2 * n_sem), *[HBM_SPEC] * n, VMEM_SPEC),
        input_output_aliases={i: 2 * n_sem + i for i in range(n)},
        compiler_params=pltpu.CompilerParams(has_side_effects=EFFECT),
    )(*[_hbm(l) for l in lands], *after)
    return list(out[:2 * n_sem]), list(out[2 * n_sem:2 * n_sem + n]), out[-1]


def _blocks_wait(sems, lands, after, name):
    n = len(lands)
    n_sem = 3 * n

    def body(*refs):
        lnd = refs[:n]
        s_sems, r_sems = refs[n:n + n_sem], refs[n + n_sem:n + 2 * n_sem]
        x, y, c, chips = _place()
        me_chip = 2 * x + y
        for k in range(n):
            for j, (px, py) in enumerate(chips):
                cp = pltpu.make_async_remote_copy(
                    src_ref=_half_rows(lnd[k], me_chip, c), dst_ref=_half_rows(lnd[k], 2 * px + py, c),
                    send_sem=s_sems[3 * k + j], recv_sem=r_sems[3 * k + j],
                    device_id=(px, py, c), device_id_type=MESH)
                cp.wait_send()
                cp.wait_recv()

    out = pl.pallas_call(
        body, name=name,
        out_shape=tuple(pltpu.HBM(l.shape, l.dtype) for l in lands),
        in_specs=[HBM_SPEC] * n + [SEM_SPEC] * (2 * n_sem) + [ANY_SPEC] * len(after),
        out_specs=[HBM_SPEC] * n,
        input_output_aliases={i: i for i in range(n)},
        compiler_params=pltpu.CompilerParams(has_side_effects=EFFECT),
    )(*lands, *sems, *after)
    return list(out)


def _blocks_finish(lands, name):
    n = len(lands)

    def body(*refs):
        lnd = refs[n:2 * n]
        send_sems, recv_sems = refs[2 * n:]
        x, y, c, chips = _place()
        sibling = (x, y, 1 - c)

        def copy(k, j, chip_id, pc):
            return pltpu.make_async_remote_copy(
                src_ref=_half_rows(lnd[k], chip_id, pc), dst_ref=_half_rows(lnd[k], chip_id, pc),
                send_sem=send_sems.at[k, j], recv_sem=recv_sems.at[k, j], device_id=sibling, device_id_type=MESH)

        started = []
        for k in range(n):
            for j, (px, py) in enumerate(chips):
                cp = copy(k, j, 2 * px + py, c)
                cp.start()
                started.append(cp)
        for k in range(n):
            for j, (px, py) in enumerate(chips):
                copy(k, j, 2 * px + py, 1 - c).wait_recv()
        for cp in started:
            cp.wait_send()

    out = pl.pallas_call(
        body, name=name,
        out_shape=[jax.ShapeDtypeStruct(l.shape, l.dtype) for l in lands],
        in_specs=[ANY_SPEC] * n, out_specs=[ANY_SPEC] * n,
        input_output_aliases={i: i for i in range(n)},
        scratch_shapes=[pltpu.SemaphoreType.DMA((n, 3)), pltpu.SemaphoreType.DMA((n, 3))],
    )(*lands)
    return list(out)


def _gather_start(shards, name):
    n = len(shards)
    n_sem = 3 * n

    def body(*refs):
        ins, lands = refs[:n], refs[n:2 * n]
        send_sems, recv_sems = refs[2 * n:2 * n + n_sem], refs[2 * n + n_sem:2 * n + 2 * n_sem]
        token = refs[-1]
        x, y, c, chips = _place()
        me_chip = 2 * x + y
        for k in range(n):
            h = shards[k].shape[0] // 2
            rows = pl.ds(pl.multiple_of(c * h, 8), h)
            for j, chip in enumerate(chips):
                pltpu.make_async_remote_copy(
                    src_ref=ins[k].at[rows, :], dst_ref=lands[k].at[me_chip, rows, :],
                    send_sem=send_sems[3 * k + j], recv_sem=recv_sems[3 * k + j],
                    device_id=(*chip, c), device_id_type=MESH).start()
        token[...] = jnp.zeros_like(token)

    lands = [_hbm(lax.empty((N_CHIP,) + s.shape, s.dtype)) for s in shards]
    out = pl.pallas_call(
        body, name=name,
        out_shape=(*[pltpu.SemaphoreType.DMA(())] * (2 * n_sem),
                   *[pltpu.HBM(s.shape, s.dtype) for s in shards],
                   *[pltpu.HBM(l.shape, l.dtype) for l in lands],
                   jax.ShapeDtypeStruct((8, 128), F32)),
        in_specs=[HBM_SPEC] * (2 * n),
        out_specs=(*[SEM_SPEC] * (2 * n_sem), *[HBM_SPEC] * (2 * n), VMEM_SPEC),
        input_output_aliases={i: 2 * n_sem + i for i in range(2 * n)},
        compiler_params=pltpu.CompilerParams(has_side_effects=EFFECT),
    )(*[_hbm(s) for s in shards], *lands)
    sems = list(out[:2 * n_sem])
    return sems, list(out[2 * n_sem:2 * n_sem + n]), list(out[2 * n_sem + n:2 * n_sem + 2 * n]), out[-1]


def _gather_wait(sems, shards, lands, after, name):
    n = len(shards)
    n_sem = 3 * n

    def body(*refs):
        ins, lnd = refs[:n], refs[n:2 * n]
        s_sems, r_sems = refs[2 * n:2 * n + n_sem], refs[2 * n + n_sem:2 * n + 2 * n_sem]
        x, y, c, chips = _place()
        for k in range(n):
            h = shards[k].shape[0] // 2
            rows = pl.ds(pl.multiple_of(c * h, 8), h)
            for j, (px, py) in enumerate(chips):
                cp = pltpu.make_async_remote_copy(
                    src_ref=ins[k].at[rows, :], dst_ref=lnd[k].at[2 * px + py, rows, :],
                    send_sem=s_sems[3 * k + j], recv_sem=r_sems[3 * k + j],
                    device_id=(px, py, c), device_id_type=MESH)
                cp.wait_send()
                cp.wait_recv()

    out = pl.pallas_call(
        body, name=name,
        out_shape=(*[pltpu.HBM(s.shape, s.dtype) for s in shards], *[pltpu.HBM(l.shape, l.dtype) for l in lands]),
        in_specs=[HBM_SPEC] * (2 * n) + [SEM_SPEC] * (2 * n_sem) + [ANY_SPEC],
        out_specs=[HBM_SPEC] * (2 * n),
        input_output_aliases={i: i for i in range(2 * n)},
        compiler_params=pltpu.CompilerParams(has_side_effects=EFFECT),
    )(*shards, *lands, *sems, after)
    return list(out[:n]), list(out[n:])


def _gather_finish(shards, lands, name):
    n = len(shards)

    def body(*refs):
        ins, lnd = refs[:n], refs[2 * n:3 * n]
        send_sems, recv_sems, local_sems = refs[3 * n:]
        x, y, c, chips = _place()
        me_chip = 2 * x + y
        sibling = (x, y, 1 - c)

        def half(k, chip_id, pc):
            h = shards[k].shape[0] // 2
            return lnd[k].at[chip_id, pl.ds(pl.multiple_of(pc * h, 8), h), :]

        def copy(k, j, chip_id, pc):
            return pltpu.make_async_remote_copy(
                src_ref=half(k, chip_id, pc), dst_ref=half(k, chip_id, pc),
                send_sem=send_sems.at[k, j], recv_sem=recv_sems.at[k, j], device_id=sibling, device_id_type=MESH)

        locals_, started = [], []
        for k in range(n):
            cp = pltpu.make_async_copy(ins[k], lnd[k].at[me_chip], local_sems.at[k])
            cp.start()
            locals_.append(cp)
            for j, (px, py) in enumerate(chips):
                cp = copy(k, j, 2 * px + py, c)
                cp.start()
                started.append(cp)
        for k in range(n):
            for j, (px, py) in enumerate(chips):
                copy(k, j, 2 * px + py, 1 - c).wait_recv()
        for cp in started:
            cp.wait_send()
        for cp in locals_:
            cp.wait()

    out = pl.pallas_call(
        body, name=name,
        out_shape=[jax.ShapeDtypeStruct(l.shape, l.dtype) for l in lands],
        in_specs=[ANY_SPEC] * (2 * n), out_specs=[ANY_SPEC] * n,
        input_output_aliases={n + i: i for i in range(n)},
        scratch_shapes=[pltpu.SemaphoreType.DMA((n, 3)), pltpu.SemaphoreType.DMA((n, 3)),
                        pltpu.SemaphoreType.DMA((n,))],
    )(*shards, *lands)
    return list(out)


def _send_other_half(arrs, name):
    n = len(arrs)

    def body(*refs):
        ins, outs = refs[:n], refs[n:2 * n]
        send_sems, recv_sems = refs[2 * n:]
        x, y, c, _ = _place()
        cps = []
        for k in range(n):
            cp = pltpu.make_async_remote_copy(
                src_ref=_half_of(ins[k], 1 - c), dst_ref=outs[k],
                send_sem=send_sems.at[k], recv_sem=recv_sems.at[k], device_id=(x, y, 1 - c), device_id_type=MESH)
            cp.start()
            cps.append(cp)
        for cp in cps:
            cp.wait()

    return pl.pallas_call(
        body, name=name,
        out_shape=[jax.ShapeDtypeStruct(_half_shape(a.shape), a.dtype) for a in arrs],
        in_specs=[ANY_SPEC] * n, out_specs=[ANY_SPEC] * n,
        scratch_shapes=[pltpu.SemaphoreType.DMA((n,)), pltpu.SemaphoreType.DMA((n,))],
    )(*arrs)


def _send_half_start(arrs, name):
    n = len(arrs)

    def body(*refs):
        ins, lnd = refs[:n], refs[n:2 * n]
        send_sems, recv_sems = refs[2 * n:3 * n], refs[3 * n:4 * n]
        token = refs[-1]
        x, y, c, _ = _place()
        for k in range(n):
            pltpu.make_async_remote_copy(
                src_ref=_half_of(ins[k], 1 - c), dst_ref=lnd[k], send_sem=send_sems[k], recv_sem=recv_sems[k],
                device_id=(x, y, 1 - c), device_id_type=MESH).start()
        token[...] = jnp.zeros_like(token)

    lands = [_hbm(lax.empty(_half_shape(a.shape), a.dtype)) for a in arrs]
    out = pl.pallas_call(
        body, name=name,
        out_shape=(*[pltpu.SemaphoreType.DMA(())] * (2 * n), *[pltpu.HBM(a.shape, a.dtype) for a in arrs],
                   *[pltpu.HBM(l.shape, l.dtype) for l in lands], jax.ShapeDtypeStruct((8, 128), F32)),
        in_specs=[HBM_SPEC] * (2 * n),
        out_specs=(*[SEM_SPEC] * (2 * n), *[HBM_SPEC] * (2 * n), VMEM_SPEC),
        input_output_aliases={i: 2 * n + i for i in range(2 * n)},
        compiler_params=pltpu.CompilerParams(has_side_effects=EFFECT),
    )(*[_hbm(a) for a in arrs], *lands)
    return list(out[:2 * n]), list(out[2 * n:3 * n]), list(out[3 * n:4 * n]), out[-1]


def _send_half_wait(sems, arrs, lands, after, name):
    n = len(arrs)

    def body(*refs):
        ins, lnd = refs[:n], refs[n:2 * n]
        s_sems, r_sems = refs[2 * n:3 * n], refs[3 * n:4 * n]
        x, y, c, _ = _place()
        for k in range(n):
            cp = pltpu.make_async_remote_copy(
                src_ref=_half_of(ins[k], 1 - c), dst_ref=lnd[k], send_sem=s_sems[k], recv_sem=r_sems[k],
                device_id=(x, y, 1 - c), device_id_type=MESH)
            cp.wait_send()
            cp.wait_recv()

    out = pl.pallas_call(
        body, name=name,
        out_shape=tuple(pltpu.HBM(a.shape, a.dtype) for a in list(arrs) + list(lands)),
        in_specs=[HBM_SPEC] * (2 * n) + [SEM_SPEC] * (2 * n) + [ANY_SPEC] * len(after),
        out_specs=[HBM_SPEC] * (2 * n),
        input_output_aliases={i: i for i in range(2 * n)},
        compiler_params=pltpu.CompilerParams(has_side_effects=EFFECT),
    )(*arrs, *lands, *sems, *after)
    return list(out[:n]), list(out[n:])


def _blocks_to_owner(arrs, name):
    n = len(arrs)

    def body(*refs):
        ins, outs = refs[:n], refs[n:2 * n]
        send_sems, recv_sems, local_sems = refs[2 * n:]
        x, y, c, chips = _place()
        me_chip = 2 * x + y
        locals_, started = [], []
        for k in range(n):
            cp = pltpu.make_async_copy(ins[k].at[me_chip], outs[k].at[me_chip], local_sems.at[k])
            cp.start()
            locals_.append(cp)

        def copy(k, j, src_block, dst_slot, to):
            return pltpu.make_async_remote_copy(
                src_ref=ins[k].at[src_block], dst_ref=outs[k].at[dst_slot],
                send_sem=send_sems.at[k, j], recv_sem=recv_sems.at[k, j], device_id=to, device_id_type=MESH)

        for k in range(n):
            for j, (px, py) in enumerate(chips):
                cp = copy(k, j, 2 * px + py, me_chip, (px, py, c))
                cp.start()
                started.append(cp)
        for k in range(n):
            for j, (px, py) in enumerate(chips):
                copy(k, j, me_chip, 2 * px + py, (px, py, c)).wait_recv()
        for cp in started:
            cp.wait_send()
        for cp in locals_:
            cp.wait()

    return pl.pallas_call(
        body, name=name,
        out_shape=[jax.ShapeDtypeStruct(a.shape, a.dtype) for a in arrs],
        in_specs=[ANY_SPEC] * n, out_specs=[ANY_SPEC] * n,
        scratch_shapes=[pltpu.SemaphoreType.DMA((n, 3)), pltpu.SemaphoreType.DMA((n, 3)),
                        pltpu.SemaphoreType.DMA((n,))],
    )(*arrs)


def _scatter_blocks(arrs, name):
    n = len(arrs)

    def body(*refs):
        ins, outs = refs[:n], refs[n:2 * n]
        send_sems, recv_sems = refs[2 * n:]
        x, y, c, chips = _place()
        me_chip = 2 * x + y

        def copy(k, j, src_block, dst_slot, to):
            return pltpu.make_async_remote_copy(
                src_ref=ins[k].at[src_block], dst_ref=outs[k].at[dst_slot],
                send_sem=send_sems.at[k, j], recv_sem=recv_sems.at[k, j], device_id=to, device_id_type=MESH)

        started = []
        for k in range(n):
            for j, (px, py) in enumerate(chips):
                cp = copy(k, j, 2 * px + py, me_chip, (px, py, c))
                cp.start()
                started.append(cp)
        for k in range(n):
            for j, (px, py) in enumerate(chips):
                copy(k, j, me_chip, 2 * px + py, (px, py, c)).wait_recv()
        for cp in started:
            cp.wait_send()

    return pl.pallas_call(
        body, name=name,
        out_shape=[jax.ShapeDtypeStruct(a.shape, a.dtype) for a in arrs],
        in_specs=[ANY_SPEC] * n, out_specs=[ANY_SPEC] * n,
        scratch_shapes=[pltpu.SemaphoreType.DMA((n, 3)), pltpu.SemaphoreType.DMA((n, 3))],
    )(*arrs)


def _scatter_start(arrs, name, after=()):
    n = len(arrs)
    n_sem = 3 * n
    first = 2 * n + len(after)

    def body(*refs):
        ins, lnd = refs[:n], refs[n:2 * n]
        send_sems, recv_sems = refs[first:first + n_sem], refs[first + n_sem:first + 2 * n_sem]
        token = refs[-1]
        x, y, c, chips = _place()
        me_chip = 2 * x + y
        for k in range(n):
            for j, (px, py) in enumerate(chips):
                pltpu.make_async_remote_copy(
                    src_ref=ins[k].at[2 * px + py], dst_ref=lnd[k].at[me_chip],
                    send_sem=send_sems[3 * k + j], recv_sem=recv_sems[3 * k + j],
                    device_id=(px, py, c), device_id_type=MESH).start()
        token[...] = jnp.zeros_like(token)

    lands = [_hbm(lax.empty(a.shape, a.dtype)) for a in arrs]
    out = pl.pallas_call(
        body, name=name,
        out_shape=(*[pltpu.SemaphoreType.DMA(())] * (2 * n_sem),
                   *[pltpu.HBM(a.shape, a.dtype) for a in arrs], *[pltpu.HBM(a.shape, a.dtype) for a in arrs],
                   jax.ShapeDtypeStruct((8, 128), F32)),
        in_specs=[HBM_SPEC] * (2 * n) + [ANY_SPEC] * len(after),
        out_specs=(*[SEM_SPEC] * (2 * n_sem), *[HBM_SPEC] * (2 * n), VMEM_SPEC),
        input_output_aliases={i: 2 * n_sem + i for i in range(2 * n)},
        compiler_params=pltpu.CompilerParams(has_side_effects=EFFECT),
    )(*[_hbm(a) for a in arrs], *lands, *after)
    base = 2 * n_sem
    return list(out[:base]), list(out[base:base + n]), list(out[base + n:base + 2 * n]), out[-1]


def _scatter_wait(sems, arrs, lands, after, name):
    n = len(arrs)
    n_sem = 3 * n

    def body(*refs):
        ins, lnd = refs[:n], refs[n:2 * n]
        s_sems, r_sems = refs[2 * n:2 * n + n_sem], refs[2 * n + n_sem:2 * n + 2 * n_sem]
        x, y, c, chips = _place()
        for k in range(n):
            for j, (px, py) in enumerate(chips):
                cp = pltpu.make_async_remote_copy(
                    src_ref=ins[k].at[2 * px + py], dst_ref=lnd[k].at[2 * px + py],
                    send_sem=s_sems[3 * k + j], recv_sem=r_sems[3 * k + j],
                    device_id=(px, py, c), device_id_type=MESH)
                cp.wait_send()
                cp.wait_recv()

    out = pl.pallas_call(
        body, name=name,
        out_shape=tuple(pltpu.HBM(a.shape, a.dtype) for a in list(arrs) + list(lands)),
        in_specs=[HBM_SPEC] * (2 * n) + [SEM_SPEC] * (2 * n_sem) + [ANY_SPEC] * len(after),
        out_specs=[HBM_SPEC] * (2 * n),
        input_output_aliases={i: i for i in range(2 * n)},
        compiler_params=pltpu.CompilerParams(has_side_effects=EFFECT),
    )(*arrs, *lands, *sems, *after)
    return list(out[:n]), list(out[n:])


def _sum_owner(chip_arr, pairs, got, name):
    nb, h, cols = got.shape
    tr = _row_tile(h, 16, 256)

    def body(chip_ref, own_ref, a_ref, b_ref, c_ref, o_ref):
        o_ref[...] = ((own_ref[0].astype(F32) + a_ref[0].astype(F32)) + b_ref[0].astype(F32)) + c_ref[0].astype(F32)

    def slot(off):
        return pl.BlockSpec((1, tr, cols), lambda i, chip_ref: ((chip_ref[0] + off) % N_CHIP, i, 0))

    return pl.pallas_call(
        body, name=name,
        grid_spec=pltpu.PrefetchScalarGridSpec(
            num_scalar_prefetch=1, grid=(h // tr,),
            in_specs=[slot(0), slot(1), slot(2), slot(3)],
            out_specs=pl.BlockSpec((tr, cols), lambda i, chip_ref: (i, 0))),
        out_shape=jax.ShapeDtypeStruct((h, cols), F32),
        compiler_params=_cparams(dimension_semantics=("parallel",)),
    )(chip_arr, pairs, got, got, got)


def _swap_start(arrs, name, after=()):
    n = len(arrs)
    first = 2 * n + len(after)

    def body(*refs):
        ins, lnd = refs[:n], refs[n:2 * n]
        send_sems, recv_sems = refs[first:first + n], refs[first + n:first + 2 * n]
        x, y, c, _ = _place()
        for k in range(n):
            pltpu.make_async_remote_copy(
                src_ref=ins[k], dst_ref=lnd[k], send_sem=send_sems[k], recv_sem=recv_sems[k],
                device_id=(x, y, 1 - c), device_id_type=MESH).start()

    lands = [_hbm(lax.empty(a.shape, a.dtype)) for a in arrs]
    out = pl.pallas_call(
        body, name=name,
        out_shape=(*[pltpu.SemaphoreType.DMA(())] * (2 * n), *[pltpu.HBM(a.shape, a.dtype) for a in arrs],
                   *[pltpu.HBM(a.shape, a.dtype) for a in arrs]),
        in_specs=[HBM_SPEC] * (2 * n) + [ANY_SPEC] * len(after),
        out_specs=(*[SEM_SPEC] * (2 * n), *[HBM_SPEC] * (2 * n)),
        input_output_aliases={i: 2 * n + i for i in range(2 * n)},
        compiler_params=pltpu.CompilerParams(has_side_effects=EFFECT),
    )(*[_hbm(a) for a in arrs], *lands, *after)
    return list(out[:2 * n]), list(out[2 * n:3 * n]), list(out[3 * n:4 * n])


def _swap_wait(sems, arrs, lands, after, name):
    n = len(arrs)

    def body(*refs):
        ins, lnd = refs[:n], refs[n:2 * n]
        s_sems, r_sems = refs[2 * n:3 * n], refs[3 * n:4 * n]
        x, y, c, _ = _place()
        for k in range(n):
            cp = pltpu.make_async_remote_copy(
                src_ref=ins[k], dst_ref=lnd[k], send_sem=s_sems[k], recv_sem=r_sems[k],
                device_id=(x, y, 1 - c), device_id_type=MESH)
            cp.wait_send()
            cp.wait_recv()

    out = pl.pallas_call(
        body, name=name,
        out_shape=tuple(pltpu.HBM(a.shape, a.dtype) for a in list(arrs) + list(lands)),
        in_specs=[HBM_SPEC] * (2 * n) + [SEM_SPEC] * (2 * n) + [ANY_SPEC] * len(after),
        out_specs=[HBM_SPEC] * (2 * n),
        input_output_aliases={i: i for i in range(2 * n)},
        compiler_params=pltpu.CompilerParams(has_side_effects=EFFECT),
    )(*arrs, *lands, *sems, *after)
    return list(out[:n]), list(out[n:])


def _swap_with_sibling(arrs, name):
    n = len(arrs)

    def body(*refs):
        ins, outs = refs[:n], refs[n:2 * n]
        send_sems, recv_sems = refs[2 * n:]
        x, y, c, _ = _place()
        cps = []
        for k in range(n):
            cp = pltpu.make_async_remote_copy(
                src_ref=ins[k], dst_ref=outs[k], send_sem=send_sems.at[k], recv_sem=recv_sems.at[k],
                device_id=(x, y, 1 - c), device_id_type=MESH)
            cp.start()
            cps.append(cp)
        for cp in cps:
            cp.wait()

    return pl.pallas_call(
        body, name=name,
        out_shape=[jax.ShapeDtypeStruct(a.shape, a.dtype) for a in arrs],
        in_specs=[ANY_SPEC] * n, out_specs=[ANY_SPEC] * n,
        scratch_shapes=[pltpu.SemaphoreType.DMA((n,)), pltpu.SemaphoreType.DMA((n,))],
    )(*arrs)


def _row_tile(h, mult=8, cap=128):
    for t in range(cap - cap % mult, mult - 1, -mult):
        if h % t == 0:
            return t
    if mult > 8:
        return _row_tile(h, 8, cap)
    raise ValueError(h)


def _cast_bf16(a, name):
    rows, cols = a.shape
    tr = _row_tile(rows, 16, 256)

    def body(a_ref, o_ref):
        o_ref[...] = a_ref[...].astype(BF16)

    return pl.pallas_call(
        body, name=name, grid=(rows // tr,),
        out_shape=jax.ShapeDtypeStruct(a.shape, BF16),
        in_specs=[pl.BlockSpec((tr, cols), lambda i: (i, 0))],
        out_specs=pl.BlockSpec((tr, cols), lambda i: (i, 0)),
        compiler_params=_cparams(dimension_semantics=("parallel",)),
    )(a)


def _pair_sum(c_arr, full, recv, name):
    nb, rows, cols = full.shape

    def body(c_ref, f_ref, r_ref, o_ref):
        o_ref[...] = (f_ref[...] + r_ref[...]).astype(BF16)

    if _halved_by_rows(full.shape):
        h = rows // 2
        tr = _row_tile(h, 16, 256)
        steps = h // tr
        own = pl.BlockSpec((1, tr, cols), lambda b, i, c_ref: (b, c_ref[0] * steps + i, 0))
        half = pl.BlockSpec((1, tr, cols), lambda b, i, c_ref: (b, i, 0))
    else:
        steps = 1
        own = pl.BlockSpec((1, rows, cols // 2), lambda b, i, c_ref: (b, 0, c_ref[0]))
        half = pl.BlockSpec((1, rows, cols // 2), lambda b, i, c_ref: (b, 0, 0))
    return pl.pallas_call(
        body, name=name,
        grid_spec=pltpu.PrefetchScalarGridSpec(
            num_scalar_prefetch=1, grid=(nb, steps), in_specs=[own, half], out_specs=half),
        out_shape=jax.ShapeDtypeStruct(_half_shape(full.shape), BF16),
        compiler_params=_cparams(dimension_semantics=("parallel", "parallel")),
    )(c_arr, full, recv)


def _sum_chips(got, name):
    nb, h, cols = got.shape
    tr = _row_tile(h, 16, 256)

    def body(g_ref, o_ref):
        g = g_ref[...].astype(F32)
        o_ref[...] = ((g[0] + g[1]) + g[2]) + g[3]

    return pl.pallas_call(
        body, name=name, grid=(h // tr,),
        out_shape=jax.ShapeDtypeStruct((h, cols), F32),
        in_specs=[pl.BlockSpec((nb, tr, cols), lambda i: (0, i, 0))],
        out_specs=pl.BlockSpec((tr, cols), lambda i: (i, 0)),
        compiler_params=_cparams(dimension_semantics=("parallel",)),
    )(got)


def _adam_math(g, w, m, v):
    m1 = ADAM_B1 * m + (1.0 - ADAM_B1) * g
    v1 = ADAM_B2 * v + (1.0 - ADAM_B2) * (g * g)
    m_hat = m1 / (1.0 - ADAM_B1 ** ADAM_STEP)
    v_hat = v1 / (1.0 - ADAM_B2 ** ADAM_STEP)
    delta = -ADAM_LR * (m_hat / (jnp.sqrt(v_hat) + ADAM_EPS) + ADAM_WD * w)
    return delta, m1, v1


def _adamw_halves(c_arr, own, other, w, m, v, name):
    rows, cols = w.shape
    by_rows = own.shape[1] == cols

    def body(c_ref, own_ref, oth_ref, w_ref, m_ref, v_ref, g_out, d_out, m_out, v_out):
        if by_rows:
            g = jnp.where(pl.program_id(0) == c_ref[0], own_ref[...], oth_ref[...])
        else:
            own_, oth_ = own_ref[...], oth_ref[...]
            g = jnp.where(c_ref[0] == 0, jnp.concatenate([own_, oth_], axis=1), jnp.concatenate([oth_, own_], axis=1))
        d, m1, v1 = _adam_math(g, w_ref[...], m_ref[...], v_ref[...])
        g_out[...] = g
        d_out[...] = d
        m_out[...] = m1
        v_out[...] = v1

    if by_rows:
        h = rows // 2
        tr = _row_tile(h)
        steps = h // tr
        grid = (2, steps)
        half_spec = pl.BlockSpec((tr, cols), lambda p, i, c_ref: (i, 0))
        full_spec = pl.BlockSpec((tr, cols), lambda p, i, c_ref: (p * steps + i, 0))
    else:
        tr = _row_tile(rows)
        grid = (1, rows // tr)
        half_spec = pl.BlockSpec((tr, cols // 2), lambda p, i, c_ref: (i, 0))
        full_spec = pl.BlockSpec((tr, cols), lambda p, i, c_ref: (i, 0))
    return pl.pallas_call(
        body, name=name,
        grid_spec=pltpu.PrefetchScalarGridSpec(
            num_scalar_prefetch=1, grid=grid,
            in_specs=[half_spec, half_spec, full_spec, full_spec, full_spec],
            out_specs=[full_spec] * 4),
        out_shape=[jax.ShapeDtypeStruct(w.shape, F32)] * 4,
        compiler_params=_cparams(dimension_semantics=("parallel", "parallel")),
    )(c_arr, own, other, w, m, v)


def _adamw_whole(items, name):
    n = len(items)

    def body(*refs):
        ins, outs = refs[:4 * n], refs[4 * n:]
        for k in range(n):
            g, w, m, v = (r[...] for r in ins[4 * k:4 * k + 4])
            d, m1, v1 = _adam_math(g, w, m, v)
            outs[3 * k][...] = d
            outs[3 * k + 1][...] = m1
            outs[3 * k + 2][...] = v1

    flat = [a for it in items for a in it]
    shapes = [jax.ShapeDtypeStruct(it[1].shape, F32) for it in items for _ in range(3)]
    out = pl.pallas_call(
        body, name=name, out_shape=shapes,
        in_specs=[VMEM_SPEC] * (4 * n), out_specs=[VMEM_SPEC] * (3 * n),
        compiler_params=_cparams(),
    )(*flat)
    return [tuple(out[3 * k:3 * k + 3]) for k in range(n)]


def _adamw_tiled(g, w, m, v, name):
    rows, cols = w.shape
    tr = _row_tile(rows)

    def body(g_ref, w_ref, m_ref, v_ref, d_out, m_out, v_out):
        d, m1, v1 = _adam_math(g_ref[...], w_ref[...], m_ref[...], v_ref[...])
        d_out[...] = d
        m_out[...] = m1
        v_out[...] = v1

    spec = pl.BlockSpec((tr, cols), lambda i: (i, 0))
    return pl.pallas_call(
        body, name=name, grid=(rows // tr,),
        out_shape=[jax.ShapeDtypeStruct(w.shape, F32)] * 3,
        in_specs=[spec] * 4, out_specs=[spec] * 3,
        compiler_params=_cparams(dimension_semantics=("parallel",)),
    )(g, w, m, v)


def _mod_forward(cond, w_mod, b_mod_cols, name):
    def body(c_ref, w_ref, b_ref, o_ref):
        o_ref[...] = _dot(_silu(c_ref[...]), w_ref[...]) + b_ref[...]

    return pl.pallas_call(
        body, name=name, out_shape=jax.ShapeDtypeStruct((cond.shape[0], w_mod.shape[1]), F32),
        in_specs=[VMEM_SPEC] * 3, out_specs=VMEM_SPEC, compiler_params=_cparams(),
    )(cond, w_mod, b_mod_cols)


def _mod_backward(cond, w_mod, dmod_cols, name):
    def body(c_ref, w_ref, d_ref, gw_ref, gc_ref):
        s = _silu(c_ref[...])
        d = d_ref[...]
        gw_ref[...] = _dot_tn(s, d)
        gc_ref[...] = _dot_nt(d[8:16, :], w_ref[...])

    return pl.pallas_call(
        body, name=name,
        out_shape=[jax.ShapeDtypeStruct(w_mod.shape, F32), jax.ShapeDtypeStruct((8, w_mod.shape[0]), F32)],
        in_specs=[VMEM_SPEC] * 3, out_specs=[VMEM_SPEC] * 2, compiler_params=_cparams(),
    )(cond, w_mod, dmod_cols)


def _col_chunks(width, step=512):
    return [(s, min(step, width - s)) for s in range(0, width, step)]


def _w_in_row(p_off):
    if p_off < 9 * HW:
        return p_off
    return 9 * HW if p_off == OFF_LR else p_off + 2 * RANK


def _in_projection(ctx0, x0, modc, modx, pre1, w_t, n_ctx_tiles, name):
    d = x0.shape[1]
    rows = ctx0.shape[0] + x0.shape[0]
    width = P_WIDTH

    def body(ctx_ref, x_ref, modc_ref, modx_ref, pre_ref, w_ref, h_ref, p_ref):
        is_ctx = pl.program_id(0) < n_ctx_tiles
        n, _ = _rms(jnp.where(is_ctx, ctx_ref[...], x_ref[...]))
        shift = jnp.where(is_ctx, modc_ref[0:1, :], modx_ref[0:1, :])
        scale = jnp.where(is_ctx, modc_ref[1:2, :], modx_ref[1:2, :])
        h = (n * pre_ref[...] * (1.0 + scale) + shift).astype(BF16)
        h_ref[...] = h
        for s, w in _col_chunks(width):
            p_ref[:, s:s + w] = _dot_nt(h, w_ref[_w_in_row(s):_w_in_row(s) + w, :])

    row = lambda i: (i, 0)
    fixed = lambda i: (0, 0)
    return pl.pallas_call(
        body, name=name, grid=(rows // TM,),
        out_shape=[jax.ShapeDtypeStruct((rows, d), BF16), jax.ShapeDtypeStruct((rows, width), F32)],
        in_specs=[pl.BlockSpec((TM, d), lambda i: (jnp.minimum(i, n_ctx_tiles - 1), 0)),
                  pl.BlockSpec((TM, d), lambda i: (jnp.maximum(i - n_ctx_tiles, 0), 0)),
                  pl.BlockSpec((8, d), fixed), pl.BlockSpec((8, d), fixed), pl.BlockSpec((1, d), fixed), VMEM_SPEC],
        out_specs=[pl.BlockSpec((TM, d), row), pl.BlockSpec((TM, width), row)],
        compiler_params=_cparams(dimension_semantics=("parallel",)),
    )(ctx0, x0, modc, modx, pre1, w_t)


C_HQ, C_HI, C_HF_FW, C_HF_BW, C_HGATE, C_GQ, C_GK, C_GV, C_GGATE = range(9)
OFF_GATE_HG = 9 * HW
OFF_LR = 13 * HW
P_WIDTH = OFF_LR + 128


def _head_norm_fwd(o, w):
    outs, ns, rs = [], [], []
    for h in range(NH):
        n, r = _rms(o[:, h * HD:(h + 1) * HD])
        ns.append(n)
        rs.append(r)
        outs.append(n * w)
    return jnp.concatenate(outs, axis=1), ns, rs


def _mixer_tail(z, o_hg, o_gla, p_hgate, p_ggate, p_gate_hg, p_gate_gla, hg_on, gla_on, wbh, wbg, wout):
    on_hg, n_hg, r_hg = _head_norm_fwd(o_hg, hg_on)
    on_gla, n_gla, r_gla = _head_norm_fwd(o_gla, gla_on)
    og_hg = (on_hg * _silu(p_hgate)).astype(BF16)
    og_gla = (on_gla * _silu(p_ggate)).astype(BF16)
    b_hg = jnp.dot(og_hg, wbh, preferred_element_type=F32)
    b_gla = jnp.dot(og_gla, wbg, preferred_element_type=F32)
    s_hg = _sigmoid(p_gate_hg)
    s_gla = _sigmoid(p_gate_gla)
    merged = (s_hg * b_hg + s_gla * b_gla).astype(BF16)
    y1 = jnp.dot(merged, wout, preferred_element_type=F32)
    return dict(on_hg=on_hg, n_hg=n_hg, r_hg=r_hg, on_gla=on_gla, n_gla=n_gla, r_gla=r_gla, og_hg=og_hg,
                og_gla=og_gla, b_hg=b_hg, b_gla=b_gla, s_hg=s_hg, s_gla=s_gla, merged=merged, y1=y1)


def _mixer_ffn(x_lat, p, o_list, modx, norms, onorms, w_br_hg, w_br_gla, w_out, w_gate, w_up, w_down, target,
               n_ctx_tiles, name):
    rows, d = x_lat.shape
    dff = w_gate.shape[0]
    inv_d = 1.0 / d

    def body(x_ref, ofw_hg, obw_hg, ofw_gla, obw_gla, p_hgate, p_ggate, p_ghg_a, p_ghg_b, p_ggla_a, p_ggla_b,
             modx_ref, norm_ref, on_ref, wbh_ref, wbg_ref, wout_ref, wg_ref, wu_ref, wd_ref, t_ref,
             loss_ref, dz2_ref, y1_ref, mrg_ref, oghg_ref, oggla_ref, h2_ref, a_ref, du_ref, dv_ref, dy2_ref,
             stat_ref):
        i = pl.program_id(0)
        post1, pre2, post2 = norm_ref[1:2, :], norm_ref[2:3, :], norm_ref[3:4, :]
        gate1, shift2, scale2, gate2 = modx_ref[2:3, :], modx_ref[3:4, :], modx_ref[4:5, :], modx_ref[5:6, :]
        p_gate_hg = jnp.concatenate([p_ghg_a[...], p_ghg_b[...]], axis=1)
        p_gate_gla = jnp.concatenate([p_ggla_a[...], p_ggla_b[...]], axis=1)
        t = _mixer_tail(x_ref[...], ofw_hg[...] + obw_hg[...], ofw_gla[...] + obw_gla[...], p_hgate[...],
                        p_ggate[...], p_gate_hg, p_gate_gla, on_ref[0:1, 0:HD], on_ref[1:2, 0:HD],
                        wbh_ref[...], wbg_ref[...], wout_ref[...])
        y1_ref[...] = t["y1"]
        mrg_ref[...] = t["merged"]
        oghg_ref[...] = t["og_hg"]
        oggla_ref[...] = t["og_gla"]
        n1, _ = _rms(t["y1"])
        z2 = x_ref[...] + n1 * post1 * gate1
        n2, r2 = _rms(z2)
        nw2 = n2 * pre2
        h2 = (nw2 * (1.0 + scale2) + shift2).astype(BF16)
        h2_ref[...] = h2
        u = _dot_nt(h2, wg_ref[...])
        v = _dot_nt(h2, wu_ref[...])
        su = _silu(u)
        a = (su * v).astype(BF16)
        a_ref[...] = a
        y2 = jnp.dot(a, wd_ref[...], preferred_element_type=F32)
        n3, r3 = _rms(y2)
        z3 = z2 + n3 * post2 * gate2
        err = z3 - t_ref[...]
        part = 0.5 * inv_d * jnp.sum(err * err)
        dz3 = err * inv_d
        dgate2 = _colsum(dz3 * n3 * post2)
        tt = dz3 * gate2
        dpost2 = _colsum(tt * n3)
        dy2 = _rms_bwd(tt * post2, n3, r3).astype(BF16)
        dy2_ref[...] = dy2
        da = _dot_nt(dy2, wd_ref[...])
        du = (da * v * _dsilu(u)).astype(BF16)
        dv = (da * su).astype(BF16)
        du_ref[...] = du
        dv_ref[...] = dv
        dh2 = (jnp.dot(du, wg_ref[...], preferred_element_type=F32)
               + jnp.dot(dv, wu_ref[...], preferred_element_type=F32))
        dshift2 = _colsum(dh2)
        dscale2 = _colsum(dh2 * nw2)
        dnw2 = dh2 * (1.0 + scale2)
        dpre2 = _colsum(dnw2 * n2)
        dz2_ref[...] = dz3 + _rms_bwd(dnw2 * pre2, n2, r2)

        @pl.when(i == 0)
        def _():
            stat_ref[...] = jnp.zeros_like(stat_ref)
            loss_ref[...] = jnp.zeros_like(loss_ref)

        for r, val in enumerate((dshift2, dscale2, dgate2, dpre2, dpost2)):
            stat_ref[r:r + 1, :] += val
        loss_ref[...] += part
        stat_ref[5:6, 0:128] += part

    tm = TM_FFN
    ctx_tiles = n_ctx_tiles * (TM // tm)
    lat = lambda i: (i, 0)
    full = lambda i: (i + ctx_tiles, 0)
    fixed = lambda i: (0, 0)

    def pcol(blk):
        return pl.BlockSpec((tm, HW), lambda i: (i + ctx_tiles, blk))

    in_specs = ([pl.BlockSpec((tm, d), lat)] + [pl.BlockSpec((tm, HW), full)] * 4
                + [pcol(C_HGATE), pcol(C_GGATE), pcol(9), pcol(10), pcol(11), pcol(12)]
                + [pl.BlockSpec((8, d), fixed), pl.BlockSpec((8, d), fixed), pl.BlockSpec((8, d), fixed)]
                + [VMEM_SPEC] * 6 + [pl.BlockSpec((tm, d), lat)])
    bf = lambda w: jax.ShapeDtypeStruct((rows, w), BF16)
    out_shape = [jax.ShapeDtypeStruct((8, 128), F32), jax.ShapeDtypeStruct((rows, d), F32),
                 jax.ShapeDtypeStruct((rows, d), F32), bf(d), bf(HW), bf(HW), bf(d), bf(dff), bf(dff), bf(dff), bf(d),
                 jax.ShapeDtypeStruct((8, d), F32)]
    out_specs = [pl.BlockSpec((8, 128), fixed), pl.BlockSpec((tm, d), lat), pl.BlockSpec((tm, d), lat),
                 pl.BlockSpec((tm, d), lat), pl.BlockSpec((tm, HW), lat), pl.BlockSpec((tm, HW), lat),
                 pl.BlockSpec((tm, d), lat), pl.BlockSpec((tm, dff), lat), pl.BlockSpec((tm, dff), lat),
                 pl.BlockSpec((tm, dff), lat), pl.BlockSpec((tm, d), lat), pl.BlockSpec((8, d), fixed)]
    return pl.pallas_call(
        body, name=name, grid=(rows // tm,), out_shape=out_shape, in_specs=in_specs, out_specs=out_specs,
        compiler_params=_cparams(dimension_semantics=("arbitrary",)),
    )(x_lat, *o_list, p, p, p, p, p, p, modx, norms, onorms, w_br_hg, w_br_gla, w_out, w_gate, w_up, w_down, target)


def _mixer_tail_fwd(x_lat, p, o_list, modx, norms, onorms, w_br_hg, w_br_gla, w_out, n_ctx_tiles, name):
    rows, d = x_lat.shape

    def body(x_ref, ofw_hg, obw_hg, ofw_gla, obw_gla, p_hgate, p_ggate, p_ghg_a, p_ghg_b, p_ggla_a, p_ggla_b,
             modx_ref, norm_ref, on_ref, wbh_ref, wbg_ref, wout_ref, z2_ref, y1_ref, mrg_ref, oghg_ref, oggla_ref):
        p_gate_hg = jnp.concatenate([p_ghg_a[...], p_ghg_b[...]], axis=1)
        p_gate_gla = jnp.concatenate([p_ggla_a[...], p_ggla_b[...]], axis=1)
        t = _mixer_tail(x_ref[...], ofw_hg[...] + obw_hg[...], ofw_gla[...] + obw_gla[...], p_hgate[...],
                        p_ggate[...], p_gate_hg, p_gate_gla, on_ref[0:1, 0:HD], on_ref[1:2, 0:HD],
                        wbh_ref[...], wbg_ref[...], wout_ref[...])
        y1_ref[...] = t["y1"]
        mrg_ref[...] = t["merged"]
        oghg_ref[...] = t["og_hg"]
        oggla_ref[...] = t["og_gla"]
        n1, _ = _rms(t["y1"])
        z2_ref[...] = x_ref[...] + n1 * norm_ref[1:2, :] * modx_ref[2:3, :]

    lat = lambda i: (i, 0)
    full = lambda i: (i + n_ctx_tiles, 0)
    fixed = lambda i: (0, 0)

    def pcol(blk):
        return pl.BlockSpec((TM, HW), lambda i: (i + n_ctx_tiles, blk))

    in_specs = ([pl.BlockSpec((TM, d), lat)] + [pl.BlockSpec((TM, HW), full)] * 4
                + [pcol(C_HGATE), pcol(C_GGATE), pcol(9), pcol(10), pcol(11), pcol(12)]
                + [pl.BlockSpec((8, d), fixed)] * 3 + [VMEM_SPEC] * 3)
    bf = lambda w: jax.ShapeDtypeStruct((rows, w), BF16)
    f32 = jax.ShapeDtypeStruct((rows, d), F32)
    return pl.pallas_call(
        body, name=name, grid=(rows // TM,), out_shape=[f32, f32, bf(d), bf(HW), bf(HW)], in_specs=in_specs,
        out_specs=[pl.BlockSpec((TM, d), lat)] * 3 + [pl.BlockSpec((TM, HW), lat)] * 2,
        compiler_params=_cparams(dimension_semantics=("parallel",)),
    )(x_lat, *o_list, p, p, p, p, p, p, modx, norms, onorms, w_br_hg, w_br_gla, w_out)


def _ffn_fwd_bwd(z2, modx, norms, w_gate, w_up, w_down, target, name):
    rows, d = z2.shape
    dff = w_gate.shape[0]
    inv_d = 1.0 / d

    def body(z2_ref, modx_ref, norm_ref, wg_ref, wu_ref, wd_ref, t_ref,
             loss_ref, dz2_ref, h2_ref, a_ref, du_ref, dv_ref, dy2_ref, stat_ref):
        i = pl.program_id(0)
        pre2, post2 = norm_ref[2:3, :], norm_ref[3:4, :]
        shift2, scale2, gate2 = modx_ref[3:4, :], modx_ref[4:5, :], modx_ref[5:6, :]
        z2 = z2_ref[...]
        n2, r2 = _rms(z2)
        nw2 = n2 * pre2
        h2 = (nw2 * (1.0 + scale2) + shift2).astype(BF16)
        h2_ref[...] = h2
        u = _dot_nt(h2, wg_ref[...])
        v = _dot_nt(h2, wu_ref[...])
        su = _silu(u)
        a = (su * v).astype(BF16)
        a_ref[...] = a
        y2 = jnp.dot(a, wd_ref[...], preferred_element_type=F32)
        n3, r3 = _rms(y2)
        err = z2 + n3 * post2 * gate2 - t_ref[...]
        part = 0.5 * inv_d * jnp.sum(err * err)
        dz3 = err * inv_d
        dgate2 = _colsum(dz3 * n3 * post2)
        tt = dz3 * gate2
        dpost2 = _colsum(tt * n3)
        dy2 = _rms_bwd(tt * post2, n3, r3).astype(BF16)
        dy2_ref[...] = dy2
        da = _dot_nt(dy2, wd_ref[...])
        du = (da * v * _dsilu(u)).astype(BF16)
        dv = (da * su).astype(BF16)
        du_ref[...] = du
        dv_ref[...] = dv
        dh2 = (jnp.dot(du, wg_ref[...], preferred_element_type=F32)
               + jnp.dot(dv, wu_ref[...], preferred_element_type=F32))
        dshift2 = _colsum(dh2)
        dscale2 = _colsum(dh2 * nw2)
        dnw2 = dh2 * (1.0 + scale2)
        dpre2 = _colsum(dnw2 * n2)
        dz2_ref[...] = dz3 + _rms_bwd(dnw2 * pre2, n2, r2)

        @pl.when(i == 0)
        def _():
            stat_ref[...] = jnp.zeros_like(stat_ref)
            loss_ref[...] = jnp.zeros_like(loss_ref)

        for r, val in enumerate((dshift2, dscale2, dgate2, dpre2, dpost2)):
            stat_ref[r:r + 1, :] += val
        loss_ref[...] += part
        stat_ref[5:6, 0:128] += part

    lat = lambda i: (i, 0)
    fixed = lambda i: (0, 0)
    bf = lambda w: jax.ShapeDtypeStruct((rows, w), BF16)
    return pl.pallas_call(
        body, name=name, grid=(rows // TM,),
        out_shape=[jax.ShapeDtypeStruct((8, 128), F32), jax.ShapeDtypeStruct((rows, d), F32), bf(d), bf(dff), bf(dff),
                   bf(dff), bf(d), jax.ShapeDtypeStruct((8, d), F32)],
        in_specs=[pl.BlockSpec((TM, d), lat), pl.BlockSpec((8, d), fixed), pl.BlockSpec((8, d), fixed)]
        + [VMEM_SPEC] * 3 + [pl.BlockSpec((TM, d), lat)],
        out_specs=[pl.BlockSpec((8, 128), fixed), pl.BlockSpec((TM, d), lat), pl.BlockSpec((TM, d), lat),
                   pl.BlockSpec((TM, dff), lat), pl.BlockSpec((TM, dff), lat), pl.BlockSpec((TM, dff), lat),
                   pl.BlockSpec((TM, d), lat), pl.BlockSpec((8, d), fixed)],
        compiler_params=_cparams(dimension_semantics=("arbitrary",)),
    )(z2, modx, norms, w_gate, w_up, w_down, target)


def _mixer_tail_bwd(x_lat, p, o_list, dz2, y1, modx, norms, onorms, w_br_hg, w_br_gla, w_out, n_ctx_tiles, n_tiles,
                    name):
    rows, d = x_lat.shape
    total = n_tiles * TM

    def body(x_ref, ofw_hg, obw_hg, ofw_gla, obw_gla, p_hgate, p_ggate, p_ghg_a, p_ghg_b, p_ggla_a, p_ggla_b,
             dz2_ref, y1_ref, modx_ref, norm_ref, on_ref, wbh_ref, wbg_ref, wout_ref,
             dohg_ref, dogla_ref, dhgate_ref, dggate_ref, dghg_ref, dggla_ref, dy1_ref, dbhg_ref, dbgla_ref,
             stat_ref):
        i = pl.program_id(0)

        @pl.when(i == 0)
        def _():
            stat_ref[...] = jnp.zeros_like(stat_ref)

        @pl.when(i < n_ctx_tiles)
        def _():
            for ref in (dohg_ref, dogla_ref, dhgate_ref, dggate_ref, dghg_ref, dggla_ref):
                ref[...] = jnp.zeros_like(ref)

        @pl.when(i >= n_ctx_tiles)
        def _():
            post1, gate1 = norm_ref[1:2, :], modx_ref[2:3, :]
            hg_on, gla_on = on_ref[0:1, 0:HD], on_ref[1:2, 0:HD]
            p_gate_hg = jnp.concatenate([p_ghg_a[...], p_ghg_b[...]], axis=1)
            p_gate_gla = jnp.concatenate([p_ggla_a[...], p_ggla_b[...]], axis=1)
            ph, pg = p_hgate[...], p_ggate[...]
            t = _mixer_tail(x_ref[...], ofw_hg[...] + obw_hg[...], ofw_gla[...] + obw_gla[...], ph, pg,
                            p_gate_hg, p_gate_gla, hg_on, gla_on, wbh_ref[...], wbg_ref[...], wout_ref[...])
            dz2 = dz2_ref[...]
            n1, r1 = _rms(y1_ref[...])
            dgate1 = _colsum(dz2 * n1 * post1)
            tt = dz2 * gate1
            dpost1 = _colsum(tt * n1)
            dy1 = _rms_bwd(tt * post1, n1, r1).astype(BF16)
            dy1_ref[...] = dy1
            dmerged = _dot_nt(dy1, wout_ref[...])
            dghg_ref[...] = dmerged * t["b_hg"] * t["s_hg"] * (1.0 - t["s_hg"])
            dggla_ref[...] = dmerged * t["b_gla"] * t["s_gla"] * (1.0 - t["s_gla"])
            db_hg = (dmerged * t["s_hg"]).astype(BF16)
            db_gla = (dmerged * t["s_gla"]).astype(BF16)
            dbhg_ref[...] = db_hg
            dbgla_ref[...] = db_gla
            don_acc = []
            for (db, wb, pgate, on, ns, rs, gain, gate_ref, do_ref) in (
                    (db_hg, wbh_ref, ph, t["on_hg"], t["n_hg"], t["r_hg"], hg_on, dhgate_ref, dohg_ref),
                    (db_gla, wbg_ref, pg, t["on_gla"], t["n_gla"], t["r_gla"], gla_on, dggate_ref, dogla_ref)):
                dog = _dot_nt(db, wb[...])
                gate_ref[...] = dog * on * _dsilu(pgate)
                don = dog * _silu(pgate)
                acc = jnp.zeros((1, HD), F32)
                for h in range(NH):
                    sl = slice(h * HD, (h + 1) * HD)
                    acc = acc + _colsum(don[:, sl] * ns[h])
                    do_ref[:, sl] = _rms_bwd(don[:, sl] * gain, ns[h], rs[h])
                don_acc.append(acc)
            stat_ref[0:1, :] += dgate1
            stat_ref[1:2, :] += dpost1
            stat_ref[2:3, 0:HD] += don_acc[0]
            stat_ref[2:3, HD:2 * HD] += don_acc[1]

    lat = lambda i: (jnp.maximum(i - n_ctx_tiles, 0), 0)
    full = lambda i: (i, 0)
    fixed = lambda i: (0, 0)

    def pcol(blk):
        return pl.BlockSpec((TM, HW), lambda i: (i, blk))

    in_specs = ([pl.BlockSpec((TM, d), lat)] + [pl.BlockSpec((TM, HW), full)] * 4
                + [pcol(C_HGATE), pcol(C_GGATE), pcol(9), pcol(10), pcol(11), pcol(12)]
                + [pl.BlockSpec((TM, d), lat), pl.BlockSpec((TM, d), lat)]
                + [pl.BlockSpec((8, d), fixed)] * 3 + [VMEM_SPEC] * 3)
    f = lambda w: jax.ShapeDtypeStruct((total, w), F32)
    out_shape = [f(HW), f(HW), f(HW), f(HW), f(d), f(d), jax.ShapeDtypeStruct((rows, d), BF16),
                 jax.ShapeDtypeStruct((rows, d), BF16), jax.ShapeDtypeStruct((rows, d), BF16),
                 jax.ShapeDtypeStruct((8, d), F32)]
    out_specs = ([pl.BlockSpec((TM, HW), full)] * 4 + [pl.BlockSpec((TM, d), full)] * 2
                 + [pl.BlockSpec((TM, d), lat)] * 3 + [pl.BlockSpec((8, d), fixed)])
    return pl.pallas_call(
        body, name=name, grid=(n_tiles,), out_shape=out_shape, in_specs=in_specs, out_specs=out_specs,
        compiler_params=_cparams(dimension_semantics=("arbitrary",)),
    )(x_lat, *o_list, p, p, p, p, p, p, dz2, y1, modx, norms, onorms, w_br_hg, w_br_gla, w_out)


def _in_projection_bwd(ctx0, x0, dz2, modc, modx, pre1, w_t, pieces, n_ctx_tiles, name):
    d = x0.shape[1]
    rows = ctx0.shape[0] + x0.shape[0]
    lat_rows = dz2.shape[0]
    width = P_WIDTH
    n_pieces = len(pieces)

    def body(*refs):
        ctx_ref, x_ref, dz2_ref, modc_ref, modx_ref, pre_ref, w_ref = refs[:7]
        (dhq_f, dhq_b, dhi_f, dhi_b, dhf_f, dhf_b, dhgate, dgq_f, dgq_b, dgk_f, dgk_b, dgv_f, dgv_b, dggate,
         dghg, dggla, dlr_f, dlr_b) = refs[7:7 + n_pieces]
        dp_ref, gx_ref, stat_ref = refs[7 + n_pieces:]
        i = pl.program_id(0)
        is_ctx = i < n_ctx_tiles
        z = jnp.where(is_ctx, ctx_ref[...], x_ref[...])
        sections = [
            (0, dhq_f[...] + dhq_b[...]), (HW, dhi_f[...] + dhi_b[...]), (2 * HW, dhf_f[...]), (3 * HW, dhf_b[...]),
            (4 * HW, dhgate[...]), (5 * HW, dgq_f[...] + dgq_b[...]), (6 * HW, dgk_f[...] + dgk_b[...]),
            (7 * HW, dgv_f[...] + dgv_b[...]), (8 * HW, dggate[...]),
            (9 * HW, dghg[:, 0:HW]), (10 * HW, dghg[:, HW:2 * HW]),
            (11 * HW, dggla[:, 0:HW]), (12 * HW, dggla[:, HW:2 * HW]), (OFF_LR, dlr_f[...] + dlr_b[...])]
        dh = jnp.zeros((TM, d), F32)
        for off, val in sections:
            w = val.shape[1]
            vb = val.astype(BF16)
            dp_ref[:, off:off + w] = vb
            dh = dh + jnp.dot(vb, w_ref[_w_in_row(off):_w_in_row(off) + w, :], preferred_element_type=F32)
        n, r = _rms(z)
        pre = pre_ref[...]
        scale = jnp.where(is_ctx, modc_ref[1:2, :], modx_ref[1:2, :])
        nw = n * pre
        dshift = _colsum(dh)
        dscale = _colsum(dh * nw)
        dnw = dh * (1.0 + scale)
        dpre = _colsum(dnw * n)
        gx_ref[...] = dz2_ref[...] + _rms_bwd(dnw * pre, n, r)
        zero = jnp.zeros((1, d), F32)

        @pl.when(i == 0)
        def _():
            stat_ref[...] = jnp.zeros_like(stat_ref)

        stat_ref[0:1, :] += jnp.where(is_ctx, zero, dshift)
        stat_ref[1:2, :] += jnp.where(is_ctx, zero, dscale)
        stat_ref[2:3, :] += jnp.where(is_ctx, dshift, zero)
        stat_ref[3:4, :] += jnp.where(is_ctx, dscale, zero)
        stat_ref[4:5, :] += dpre

    full = lambda i: (i, 0)
    lat = lambda i: (jnp.maximum(i - n_ctx_tiles, 0), 0)
    fixed = lambda i: (0, 0)
    piece_specs = [pl.BlockSpec((TM, a.shape[1]), full) for a in pieces]
    in_specs = [pl.BlockSpec((TM, d), lambda i: (jnp.minimum(i, n_ctx_tiles - 1), 0)), pl.BlockSpec((TM, d), lat),
                pl.BlockSpec((TM, d), lat), pl.BlockSpec((8, d), fixed),
                pl.BlockSpec((8, d), fixed), pl.BlockSpec((1, d), fixed), VMEM_SPEC] + piece_specs
    return pl.pallas_call(
        body, name=name, grid=(rows // TM,),
        out_shape=[jax.ShapeDtypeStruct((rows, width), BF16), jax.ShapeDtypeStruct((lat_rows, d), F32),
                   jax.ShapeDtypeStruct((8, d), F32)],
        in_specs=in_specs,
        out_specs=[pl.BlockSpec((TM, width), full), pl.BlockSpec((TM, d), lat), pl.BlockSpec((8, d), fixed)],
        compiler_params=_cparams(dimension_semantics=("arbitrary",)),
    )(ctx0, x0, dz2, modc, modx, pre1, w_t, *pieces)


def _transposed_lhs_matmul(x_ref, dy_ref, o_ref, xt_ref):
    @pl.when(pl.program_id(1) == 0)
    def _():
        xt_ref[...] = x_ref[...].T

    o_ref[...] = jnp.dot(xt_ref[...], dy_ref[...], preferred_element_type=F32)


def _w_in_grad(dp, h1, n_cols, name):
    rows, d = h1.shape
    n_main = OFF_LR // HW
    lr0 = _w_in_row(OFF_LR)

    def body(x_ref, xlr_ref, h_ref, o_hbm, xt_ref, acc_ref, sem):
        i = pl.program_id(0)

        def main_copy(step):
            row = jnp.where(step < 9, step * HW, step * HW + 2 * RANK)
            return pltpu.make_async_copy(acc_ref, o_hbm.at[pl.ds(pl.multiple_of(row, 8), HW), :], sem)

        lr_copy = pltpu.make_async_copy(acc_ref.at[0:2 * RANK, :], o_hbm.at[lr0:lr0 + 2 * RANK, :], sem)

        @pl.when(i < n_main)
        def _():
            xt_ref[...] = x_ref[...].T

        @pl.when(i > 0)
        def _():
            main_copy(i - 1).wait()

        @pl.when(i < n_main)
        def _():
            acc_ref[...] = jnp.dot(xt_ref[...], h_ref[...], preferred_element_type=F32)
            main_copy(i).start()

        @pl.when(i == n_main)
        def _():
            xt_ref[0:128, :] = xlr_ref[...].T
            acc_ref[0:128, :] = jnp.dot(xt_ref[0:128, :], h_ref[...], preferred_element_type=F32)
            lr_copy.start()
            lr_copy.wait()

    return pl.pallas_call(
        body, name=name, grid=(n_main + 1,),
        out_shape=jax.ShapeDtypeStruct((n_cols, d), F32),
        in_specs=[pl.BlockSpec((rows, HW), lambda i: (0, jnp.minimum(i, n_main - 1))),
                  pl.BlockSpec((rows, 128), lambda i: (0, OFF_LR // 128)),
                  pl.BlockSpec((rows, d), lambda i: (0, 0))],
        out_specs=ANY_SPEC,
        scratch_shapes=[pltpu.VMEM((HW, rows), BF16), pltpu.VMEM((HW, d), F32), pltpu.SemaphoreType.DMA],
        compiler_params=_cparams(dimension_semantics=("arbitrary",)),
    )(dp, dp, h1)


def _weight_grad(xs, dy, name, tk=None, tn=512, k_first=0, k_tiles=None):
    rows = dy.shape[0]
    n = dy.shape[1]
    tn_ = min(tn, n)
    tk_ = xs.shape[1] if tk is None else tk
    k_tiles = xs.shape[1] // tk_ if k_tiles is None else k_tiles
    k = k_tiles * tk_

    return pl.pallas_call(
        functools.partial(_transposed_lhs_matmul), name=name, grid=(k_tiles, n // tn_),
        out_shape=jax.ShapeDtypeStruct((k, n), F32),
        in_specs=[pl.BlockSpec((rows, tk_), lambda i, j: (0, i + k_first)),
                  pl.BlockSpec((rows, tn_), lambda i, j: (0, j))],
        out_specs=pl.BlockSpec((tk_, tn_), lambda i, j: (i, j)),
        scratch_shapes=[pltpu.VMEM((tk_, rows), BF16)],
        compiler_params=_cparams(dimension_semantics=("parallel", "arbitrary")),
    )(xs, dy)


def _running_sum(x, fw):
    c = x.shape[0]
    row = lax.broadcasted_iota(jnp.int32, (c, 1), 0)
    s = 1
    while s < c:
        if fw:
            x = x + jnp.where(row >= s, pltpu.roll(x, s, axis=0), 0.0)
        else:
            x = x + jnp.where(row < c - s, pltpu.roll(x, c - s, axis=0), 0.0)
        s *= 2
    return x


def _chunk_terms(q, k, g, fw):
    c = CHUNK
    r = lax.broadcasted_iota(jnp.int32, (c, c), 0)
    s = lax.broadcasted_iota(jnp.int32, (c, c), 1)
    causal = (s <= r) if fw else (s >= r)
    causal_t = (s >= r) if fw else (s <= r)
    cum = _running_sum(g, fw)
    row = lax.broadcasted_iota(jnp.int32, (c, 1), 0)
    pos = row if fw else (c - 1 - row)
    starts = [None]
    for j in range(1, NSUB):
        rj = SUB * j - 1 if fw else c - SUB * j
        starts.append(cum[rj:rj + 1, :])
    in_blk = [(pos >= SUB * j) & (pos < SUB * (j + 1)) for j in range(NSUB)]
    e = [jnp.exp(cum)]
    for j in range(1, NSUB):
        e.append(jnp.exp(jnp.where(pos >= SUB * j, cum - starts[j], -1e30)))
    own = jnp.zeros_like(cum)
    for j in range(1, NSUB):
        own = own + jnp.where(in_blk[j], starts[j], 0.0)
    kscale = jnp.exp(own - cum)
    rend = c - 1 if fw else 0
    cend = cum[rend:rend + 1, :]
    tail = jnp.exp(cend - cum)
    qcat = jnp.concatenate([q * e[j] for j in range(NSUB)], axis=1).astype(BF16)
    kt = k * kscale
    km = jnp.concatenate([jnp.where(in_blk[j], kt, 0.0) for j in range(NSUB)], axis=1).astype(BF16)
    return dict(causal=causal, causal_t=causal_t, e=e, in_blk=in_blk, kscale=kscale, cend=cend, tail=tail,
                qcat=qcat, km=km)


def _chunk_fwd(q, k, v, g, st0, fw):
    t = _chunk_terms(q, k, g, fw)
    a = jnp.where(t["causal"], _dot_nt(t["qcat"], t["km"]), 0.0)
    o = _dot(a, v) + _dot_nt(t["qcat"][:, 0:HD], st0)
    st1 = st0 * jnp.exp(t["cend"]) + _dot_tn(v, k * t["tail"])
    return o, st1


def _chunk_bwd(q, k, v, g, st0, do, dst1, fw):
    t = _chunk_terms(q, k, g, fw)
    qcat, km, e = t["qcat"], t["km"], t["e"]
    a_t = jnp.where(t["causal_t"], _dot_nt(km, qcat), 0.0)
    ktail = k * t["tail"]
    dv = _dot(a_t, do) + _dot_nt(ktail, dst1)
    da = jnp.where(t["causal"], _dot_nt(do, v), 0.0)
    da_t = jnp.where(t["causal_t"], _dot_nt(v, do), 0.0)
    dqcat = _dot(da, km)
    dq_inter = e[0] * _dot(do, st0)
    dq = dq_inter
    for j in range(NSUB):
        dq = dq + e[j] * dqcat[:, j * HD:(j + 1) * HD]
    dkm = _dot(da_t, qcat)
    dkt = jnp.zeros_like(k)
    for j in range(NSUB):
        dkt = dkt + jnp.where(t["in_blk"][j], dkm[:, j * HD:(j + 1) * HD], 0.0)
    dk_inter = _dot(v, dst1) * t["tail"]
    dk = dkt * t["kscale"] + dk_inter
    dcum = q * dq_inter - k * dk_inter
    for j in range(NSUB):
        sl = slice(j * HD, (j + 1) * HD)
        dcum = dcum + qcat[:, sl].astype(F32) * dqcat[:, sl] - km[:, sl].astype(F32) * dkm[:, sl]
    ecend = jnp.exp(t["cend"])
    end = ecend * _colsum(st0 * dst1) + _colsum(k * dk_inter)
    dg = _running_sum(dcum, not fw) + end
    dst0 = dst1 * ecend + _dot_tn(do, q * e[0])
    return dq, dk, dv, dg, dst0


def _running_sums(xs, fws):
    c = xs[0].shape[0]
    row = lax.broadcasted_iota(jnp.int32, (c, 1), 0)
    s = 1
    while s < c:
        xs = [x + (jnp.where(row >= s, pltpu.roll(x, s, axis=0), 0.0) if fw else
                   jnp.where(row < c - s, pltpu.roll(x, c - s, axis=0), 0.0)) for x, fw in zip(xs, fws)]
        s *= 2
    return xs


def _chunks_terms(qs, ks, gs, fws):
    c = CHUNK
    n = len(qs)
    r = lax.broadcasted_iota(jnp.int32, (c, c), 0)
    s = lax.broadcasted_iota(jnp.int32, (c, c), 1)
    row = lax.broadcasted_iota(jnp.int32, (c, 1), 0)
    per_dir = {}
    for fw in set(fws):
        pos = row if fw else (c - 1 - row)
        per_dir[fw] = dict(
            causal=(s <= r) if fw else (s >= r), causal_t=(s >= r) if fw else (s <= r), pos=pos,
            in_blk=[(pos >= SUB * j) & (pos < SUB * (j + 1)) for j in range(NSUB)],
            start_row=[None] + [SUB * j - 1 if fw else c - SUB * j for j in range(1, NSUB)],
            rend=c - 1 if fw else 0)
    dirs = [per_dir[fw] for fw in fws]
    cums = _running_sums(gs, fws)
    starts = [[None] + [cum[d["start_row"][j]:d["start_row"][j] + 1, :] for j in range(1, NSUB)]
              for cum, d in zip(cums, dirs)]
    es = [[jnp.exp(cum) for cum in cums]]
    for j in range(1, NSUB):
        es.append([jnp.exp(jnp.where(d["pos"] >= SUB * j, cum - st[j], -1e30)) for cum, st, d in zip(cums, starts, dirs)])
    owns = [sum(jnp.where(d["in_blk"][j], st[j], 0.0) for j in range(1, NSUB)) for st, d in zip(starts, dirs)]
    kscales = [jnp.exp(own - cum) for own, cum in zip(owns, cums)]
    cends = [cum[d["rend"]:d["rend"] + 1, :] for cum, d in zip(cums, dirs)]
    tails = [jnp.exp(cend - cum) for cend, cum in zip(cends, cums)]
    qcats = [jnp.concatenate([q * es[j][i] for j in range(NSUB)], axis=1).astype(BF16) for i, q in enumerate(qs)]
    kts = [k * ksc for k, ksc in zip(ks, kscales)]
    kms = [jnp.concatenate([jnp.where(d["in_blk"][j], kt, 0.0) for j in range(NSUB)], axis=1).astype(BF16)
           for kt, d in zip(kts, dirs)]
    e_by_lane = [[es[j][i] for j in range(NSUB)] for i in range(n)]
    return dict(dirs=dirs, e=e_by_lane, kscale=kscales, cend=cends, tail=tails, qcat=qcats, km=kms)


def _chunks_fwd(qs, ks, vs, gs, st0s, fws):
    t = _chunks_terms(qs, ks, gs, fws)
    scores = [_dot_nt(qc, km) for qc, km in zip(t["qcat"], t["km"])]
    a = [jnp.where(d["causal"], sc, 0.0) for sc, d in zip(scores, t["dirs"])]
    inter = [_dot_nt(qc[:, 0:HD], st0) for qc, st0 in zip(t["qcat"], st0s)]
    intra = [_dot(a_, v) for a_, v in zip(a, vs)]
    os_ = [x + y for x, y in zip(intra, inter)]
    upd = [_dot_tn(v, k * tl) for v, k, tl in zip(vs, ks, t["tail"])]
    st1s = [st0 * jnp.exp(ce) + u for st0, ce, u in zip(st0s, t["cend"], upd)]
    return os_, st1s


def _chunks_bwd(qs, ks, vs, gs, st0s, dos, dst1s, fws):
    n = len(qs)
    t = _chunks_terms(qs, ks, gs, fws)
    qcat, km, e, dirs = t["qcat"], t["km"], t["e"], t["dirs"]
    a_t = [jnp.where(d["causal_t"], _dot_nt(km_, qc), 0.0) for km_, qc, d in zip(km, qcat, dirs)]
    ktail = [k * tl for k, tl in zip(ks, t["tail"])]
    dv_a = [_dot(at, do) for at, do in zip(a_t, dos)]
    dv_b = [_dot_nt(kt, ds) for kt, ds in zip(ktail, dst1s)]
    dv = [x + y for x, y in zip(dv_a, dv_b)]
    da = [jnp.where(d["causal"], _dot_nt(do, v), 0.0) for do, v, d in zip(dos, vs, dirs)]
    da_t = [jnp.where(d["causal_t"], _dot_nt(v, do), 0.0) for do, v, d in zip(dos, vs, dirs)]
    dqcat = [_dot(da_, km_) for da_, km_ in zip(da, km)]
    dq_inter = [e[i][0] * _dot(dos[i], st0s[i]) for i in range(n)]
    dkm = [_dot(dat, qc) for dat, qc in zip(da_t, qcat)]
    dk_inter = [_dot(v, ds) * tl for v, ds, tl in zip(vs, dst1s, t["tail"])]
    dq = [dq_inter[i] + sum(e[i][j] * dqcat[i][:, j * HD:(j + 1) * HD] for j in range(NSUB)) for i in range(n)]
    dkt = [sum(jnp.where(dirs[i]["in_blk"][j], dkm[i][:, j * HD:(j + 1) * HD], 0.0) for j in range(NSUB))
           for i in range(n)]
    dk = [dkt[i] * t["kscale"][i] + dk_inter[i] for i in range(n)]
    dcum = [qs[i] * dq_inter[i] - ks[i] * dk_inter[i]
            + sum(qcat[i][:, j * HD:(j + 1) * HD].astype(F32) * dqcat[i][:, j * HD:(j + 1) * HD]
                  - km[i][:, j * HD:(j + 1) * HD].astype(F32) * dkm[i][:, j * HD:(j + 1) * HD] for j in range(NSUB))
            for i in range(n)]
    ecend = [jnp.exp(ce) for ce in t["cend"]]
    end = [ecend[i] * _colsum(st0s[i] * dst1s[i]) + _colsum(ks[i] * dk_inter[i]) for i in range(n)]
    sums = _running_sums(dcum, [not fw for fw in fws])
    dg = [sm + en for sm, en in zip(sums, end)]
    upd = [_dot_tn(dos[i], qs[i] * e[i][0]) for i in range(n)]
    dst0 = [dst1s[i] * ecend[i] + upd[i] for i in range(n)]
    return dq, dk, dv, dg, dst0


def _chunk_index(step, n_ctx_chunks, n_chunks, fw):
    if fw:
        return step
    return jnp.where(step < n_ctx_chunks, n_ctx_chunks - 1 - step, n_chunks - 1 + n_ctx_chunks - step)


def _hg_inputs(hq, hf, lbv, d_idx, sl):
    lb = _sigmoid(lbv[d_idx:d_idx + 1, sl] - lbv[2 + d_idx:3 + d_idx, sl])
    sg = _sigmoid(hf)
    f = lb + (1.0 - lb) * sg
    return _silu(hq), 1.0 - f, jnp.log(f), f, sg, lb


def _scan_fwd(p, side, n_ctx_chunks, fw, branch, name):
    rows = p.shape[0]
    n_chunks = rows // CHUNK
    d_idx = 0 if fw else 1
    hg = branch == "hg"
    cols = (C_HQ, C_HI, C_HF_FW + d_idx) if hg else (C_GQ, C_GK, C_GV)

    def body(*refs):
        if hg:
            a_ref, b_ref, c_ref, lb_ref, o_ref, st_ref, state = refs
        else:
            a_ref, b_ref, c_ref, lr_ref, wgk_ref, bgk_ref, o_ref, st_ref, state = refs
            logits = _dot(lr_ref[...], wgk_ref[...]) + bgk_ref[...]
            g_all = _log_sigmoid(logits) * (1.0 / GATE_NORM)

        @pl.when(pl.program_id(0) == 0)
        def _():
            state[...] = jnp.zeros_like(state)

        for h in range(NH):
            sl = slice(h * HD, (h + 1) * HD)
            if hg:
                q, k, g, _, _, _ = _hg_inputs(a_ref[:, sl], c_ref[:, sl], lb_ref[...], d_idx, sl)
                v = b_ref[:, sl]
            else:
                q, k, v, g = a_ref[:, sl] * (HD ** -0.5), b_ref[:, sl], c_ref[:, sl], g_all[:, sl]
            st0 = state[h]
            st_ref[0, h] = st0
            o, st1 = _chunk_fwd(q, k, v, g, st0, fw)
            o_ref[:, sl] = o
            state[h] = st1

    def cmap(blk):
        return pl.BlockSpec((CHUNK, HW), lambda j: (_chunk_index(j, n_ctx_chunks, n_chunks, fw), blk))

    fixed = lambda j: (0, 0)
    in_specs = [cmap(cols[0]), cmap(cols[1]), cmap(cols[2])]
    if hg:
        in_specs += [pl.BlockSpec((4, HW), fixed)]
        args = (p, p, p, side)
    else:
        in_specs += [pl.BlockSpec((CHUNK, 128), lambda j: (_chunk_index(j, n_ctx_chunks, n_chunks, fw), OFF_LR // 128)),
                     pl.BlockSpec((128, HW), fixed), pl.BlockSpec((1, HW), fixed)]
        args = (p, p, p, p, side[0], side[1])
    return pl.pallas_call(
        body, name=name, grid=(n_chunks,),
        out_shape=[jax.ShapeDtypeStruct((rows, HW), F32), jax.ShapeDtypeStruct((n_chunks, NH, HD, HD), F32)],
        in_specs=in_specs,
        out_specs=[pl.BlockSpec((CHUNK, HW), lambda j: (_chunk_index(j, n_ctx_chunks, n_chunks, fw), 0)),
                   pl.BlockSpec((1, NH, HD, HD), lambda j: (_chunk_index(j, n_ctx_chunks, n_chunks, fw), 0, 0, 0))],
        scratch_shapes=[pltpu.VMEM((NH, HD, HD), F32)],
        compiler_params=_cparams(dimension_semantics=("arbitrary",)),
    )(*args)


def _scan_fwd_both(p, side, n_ctx_chunks, branch, name):
    rows = p.shape[0]
    n_chunks = rows // CHUNK
    hg = branch == "hg"
    n_in = 4 if hg else 6

    def body(*refs):
        ins, outs, state = refs[:2 * n_in], refs[2 * n_in:2 * n_in + 4], refs[-1]

        @pl.when(pl.program_id(0) == 0)
        def _():
            state[...] = jnp.zeros_like(state)

        lanes, where = [], []
        for di, fw in enumerate((True, False)):
            r = ins[di * n_in:(di + 1) * n_in]
            o_ref, st_ref = outs[2 * di], outs[2 * di + 1]
            if hg:
                a_ref, b_ref, c_ref, lb_ref = r
            else:
                a_ref, b_ref, c_ref, lr_ref, wgk_ref, bgk_ref = r
                logits = _dot(lr_ref[...], wgk_ref[...]) + bgk_ref[...]
                g_all = _log_sigmoid(logits) * (1.0 / GATE_NORM)
            for h in range(NH):
                sl = slice(h * HD, (h + 1) * HD)
                if hg:
                    q, k, g, _, _, _ = _hg_inputs(a_ref[:, sl], c_ref[:, sl], lb_ref[...], di, sl)
                    v = b_ref[:, sl]
                else:
                    q, k, v, g = a_ref[:, sl] * (HD ** -0.5), b_ref[:, sl], c_ref[:, sl], g_all[:, sl]
                lanes.append((q, k, v, g, state[di, h], fw))
                where.append((di, h, sl, o_ref, st_ref))
        qs, ks, vs, gs, st0s, fws = (list(col) for col in zip(*lanes))
        os_, st1s = _chunks_fwd(qs, ks, vs, gs, st0s, fws)
        for (di, h, sl, o_ref, st_ref), st0, o, st1 in zip(where, st0s, os_, st1s):
            st_ref[0, h] = st0
            o_ref[:, sl] = o
            state[di, h] = st1

    fixed = lambda j: (0, 0)
    in_specs, args, out_specs = [], [], []
    for di, fw in enumerate((True, False)):
        chunk = functools.partial(_chunk_index, n_ctx_chunks=n_ctx_chunks, n_chunks=n_chunks, fw=fw)

        def cmap(blk, width=HW, chunk=chunk):
            return pl.BlockSpec((CHUNK, width), lambda j: (chunk(j), blk))

        if hg:
            in_specs += [cmap(C_HQ), cmap(C_HI), cmap(C_HF_FW + di), pl.BlockSpec((4, HW), fixed)]
            args += [p, p, p, side]
        else:
            in_specs += [cmap(C_GQ), cmap(C_GK), cmap(C_GV), cmap(OFF_LR // 128, 128),
                         pl.BlockSpec((128, HW), fixed), pl.BlockSpec((1, HW), fixed)]
            args += [p, p, p, p, side[di][0], side[di][1]]
        out_specs += [cmap(0), pl.BlockSpec((1, NH, HD, HD), lambda j, chunk=chunk: (chunk(j), 0, 0, 0))]
    return pl.pallas_call(
        body, name=name, grid=(n_chunks,),
        out_shape=[jax.ShapeDtypeStruct((rows, HW), F32), jax.ShapeDtypeStruct((n_chunks, NH, HD, HD), F32)] * 2,
        in_specs=in_specs, out_specs=out_specs,
        scratch_shapes=[pltpu.VMEM((2, NH, HD, HD), F32)],
        compiler_params=_cparams(dimension_semantics=("arbitrary",)),
    )(*args)


def _scan_bwd_both(p, side, states, d_o, n_ctx_chunks, branch, name):
    rows = p.shape[0]
    n_chunks = rows // CHUNK
    hg = branch == "hg"
    n_in = 6 if hg else 8
    n_out = 4 if hg else 6

    def body(*refs):
        ins, outs, dstate = refs[:2 * n_in], refs[2 * n_in:2 * n_in + 2 * n_out], refs[-1]
        first = pl.program_id(0) == 0

        @pl.when(first)
        def _():
            dstate[...] = jnp.zeros_like(dstate)

        lanes, where, extra, ctx = [], [], [], []
        for di, fw in enumerate((True, False)):
            r, w = ins[di * n_in:(di + 1) * n_in], outs[di * n_out:(di + 1) * n_out]
            if hg:
                a_ref, b_ref, c_ref, lb_ref, st_ref, do_ref = r
                da_ref, db_ref, dc_ref, dlb_ref = w
                acc_refs = (dlb_ref,)
            else:
                a_ref, b_ref, c_ref, lr_ref, wgk_ref, bgk_ref, st_ref, do_ref = r
                da_ref, db_ref, dc_ref, dlr_ref, dwgk_ref, dbias_ref = w
                acc_refs = (dwgk_ref, dbias_ref)
                lr = lr_ref[...]
                logits = _dot(lr, wgk_ref[...]) + bgk_ref[...]
                g_all = _log_sigmoid(logits) * (1.0 / GATE_NORM)

            @pl.when(first)
            def _(acc_refs=acc_refs):
                for ref in acc_refs:
                    ref[...] = jnp.zeros_like(ref)

            for h in range(NH):
                sl = slice(h * HD, (h + 1) * HD)
                if hg:
                    hq, hf = a_ref[:, sl], c_ref[:, sl]
                    q, k, g, f, sg, lb = _hg_inputs(hq, hf, lb_ref[...], di, sl)
                    v = b_ref[:, sl]
                    extra.append((hq, f, sg, lb))
                else:
                    q, k, v, g = a_ref[:, sl] * (HD ** -0.5), b_ref[:, sl], c_ref[:, sl], g_all[:, sl]
                    extra.append(None)
                lanes.append((q, k, v, g, st_ref[0, h], do_ref[:, sl], dstate[di, h], fw))
                where.append((di, h, sl))
            ctx.append((w, None if hg else (lr, logits, wgk_ref)))

        qs, ks, vs, gs, st0s, dos, dst1s, fws = (list(col) for col in zip(*lanes))
        dqs, dks, dvs, dgs, dst0s = _chunks_bwd(qs, ks, vs, gs, st0s, dos, dst1s, fws)
        dg_parts = {0: [], 1: []}
        for (di, h, sl), ex, dq, dk, dv, dg, dst0 in zip(where, extra, dqs, dks, dvs, dgs, dst0s):
            dstate[di, h] = dst0
            w = ctx[di][0]
            if hg:
                hq, f, sg, lb = ex
                da_ref, db_ref, dc_ref, dlb_ref = w
                da_ref[:, sl] = dq * _dsilu(hq)
                db_ref[:, sl] = dv
                df = dg / f - dk
                dc_ref[:, sl] = df * (1.0 - lb) * sg * (1.0 - sg)
                dlb_ref[0:1, sl] += _colsum(df * (1.0 - sg))
            else:
                da_ref, db_ref, dc_ref = w[:3]
                da_ref[:, sl] = dq * (HD ** -0.5)
                db_ref[:, sl] = dk
                dc_ref[:, sl] = dv
                dg_parts[di].append(dg)
        if not hg:
            for di in range(2):
                dlr_ref, dwgk_ref, dbias_ref = ctx[di][0][3:]
                lr, logits, wgk_ref = ctx[di][1]
                dlogits = jnp.concatenate(dg_parts[di], axis=1) * (1.0 / GATE_NORM) * (1.0 - _sigmoid(logits))
                dlr_ref[...] = _dot_nt(dlogits, wgk_ref[...])
                dwgk_ref[...] += _dot_tn(lr, dlogits)
                dbias_ref[0:1, :] += _colsum(dlogits)

    fixed = lambda j: (0, 0)
    big = jax.ShapeDtypeStruct((rows, HW), F32)
    in_specs, args, out_shape, out_specs = [], [], [], []
    for di, fw in enumerate((True, False)):
        def chunk_of(j, fw=fw):
            return _chunk_index(n_chunks - 1 - j, n_ctx_chunks, n_chunks, fw)

        def cmap(blk, width=HW, chunk_of=chunk_of):
            return pl.BlockSpec((CHUNK, width), lambda j: (chunk_of(j), blk))

        st_spec = pl.BlockSpec((1, NH, HD, HD), lambda j, chunk_of=chunk_of: (chunk_of(j), 0, 0, 0))
        if hg:
            in_specs += [cmap(C_HQ), cmap(C_HI), cmap(C_HF_FW + di), pl.BlockSpec((4, HW), fixed), st_spec, cmap(0)]
            args += [p, p, p, side, states[di], d_o]
            out_shape += [big, big, big, jax.ShapeDtypeStruct((8, HW), F32)]
            out_specs += [cmap(0), cmap(0), cmap(0), pl.BlockSpec((8, HW), fixed)]
        else:
            in_specs += [cmap(C_GQ), cmap(C_GK), cmap(C_GV), cmap(OFF_LR // 128, 128),
                         pl.BlockSpec((128, HW), fixed), pl.BlockSpec((1, HW), fixed), st_spec, cmap(0)]
            args += [p, p, p, p, side[di][0], side[di][1], states[di], d_o]
            out_shape += [big, big, big, jax.ShapeDtypeStruct((rows, 128), F32),
                          jax.ShapeDtypeStruct((128, HW), F32), jax.ShapeDtypeStruct((8, HW), F32)]
            out_specs += [cmap(0), cmap(0), cmap(0), cmap(0, 128), pl.BlockSpec((128, HW), fixed),
                          pl.BlockSpec((8, HW), fixed)]
    return pl.pallas_call(
        body, name=name, grid=(n_chunks,), out_shape=out_shape, in_specs=in_specs, out_specs=out_specs,
        scratch_shapes=[pltpu.VMEM((2, NH, HD, HD), F32)],
        compiler_params=_cparams(dimension_semantics=("arbitrary",)),
    )(*args)


def _scan_bwd(p, side, states, d_o, n_ctx_chunks, fw, branch, name):
    rows = p.shape[0]
    n_chunks = rows // CHUNK
    d_idx = 0 if fw else 1
    hg = branch == "hg"
    cols = (C_HQ, C_HI, C_HF_FW + d_idx) if hg else (C_GQ, C_GK, C_GV)

    def body(*refs):
        if hg:
            a_ref, b_ref, c_ref, lb_ref, st_ref, do_ref, da_ref, db_ref, dc_ref, dlb_ref, dstate = refs
        else:
            (a_ref, b_ref, c_ref, lr_ref, wgk_ref, bgk_ref, st_ref, do_ref, da_ref, db_ref, dc_ref, dlr_ref,
             dwgk_ref, dbias_ref, dstate) = refs
            lr = lr_ref[...]
            logits = _dot(lr, wgk_ref[...]) + bgk_ref[...]
            g_all = _log_sigmoid(logits) * (1.0 / GATE_NORM)

        @pl.when(pl.program_id(0) == 0)
        def _():
            dstate[...] = jnp.zeros_like(dstate)
            if hg:
                dlb_ref[...] = jnp.zeros_like(dlb_ref)
            else:
                dwgk_ref[...] = jnp.zeros_like(dwgk_ref)
                dbias_ref[...] = jnp.zeros_like(dbias_ref)

        dg_parts = []
        for h in range(NH):
            sl = slice(h * HD, (h + 1) * HD)
            if hg:
                hq, hf = a_ref[:, sl], c_ref[:, sl]
                q, k, g, f, sg, lb = _hg_inputs(hq, hf, lb_ref[...], d_idx, sl)
                v = b_ref[:, sl]
            else:
                q, k, v, g = a_ref[:, sl] * (HD ** -0.5), b_ref[:, sl], c_ref[:, sl], g_all[:, sl]
            dq, dk, dv, dg, dst0 = _chunk_bwd(q, k, v, g, st_ref[0, h], do_ref[:, sl], dstate[h], fw)
            dstate[h] = dst0
            if hg:
                da_ref[:, sl] = dq * _dsilu(hq)
                db_ref[:, sl] = dv
                df = dg / f - dk
                dc_ref[:, sl] = df * (1.0 - lb) * sg * (1.0 - sg)
                dlb_ref[0:1, sl] += _colsum(df * (1.0 - sg))
            else:
                da_ref[:, sl] = dq * (HD ** -0.5)
                db_ref[:, sl] = dk
                dc_ref[:, sl] = dv
                dg_parts.append(dg)
        if not hg:
            dlogits = jnp.concatenate(dg_parts, axis=1) * (1.0 / GATE_NORM) * (1.0 - _sigmoid(logits))
            dlr_ref[...] = _dot_nt(dlogits, wgk_ref[...])
            dwgk_ref[...] += _dot_tn(lr, dlogits)
            dbias_ref[0:1, :] += _colsum(dlogits)

    def chunk_of(j):
        return _chunk_index(n_chunks - 1 - j, n_ctx_chunks, n_chunks, fw)

    def cmap(blk, width=HW):
        return pl.BlockSpec((CHUNK, width), lambda j: (chunk_of(j), blk))

    fixed = lambda j: (0, 0)
    st_spec = pl.BlockSpec((1, NH, HD, HD), lambda j: (chunk_of(j), 0, 0, 0))
    big = jax.ShapeDtypeStruct((rows, HW), F32)
    if hg:
        in_specs = [cmap(cols[0]), cmap(cols[1]), cmap(cols[2]), pl.BlockSpec((4, HW), fixed), st_spec, cmap(0)]
        args = (p, p, p, side, states, d_o)
        out_shape = [big, big, big, jax.ShapeDtypeStruct((8, HW), F32)]
        out_specs = [cmap(0), cmap(0), cmap(0), pl.BlockSpec((8, HW), fixed)]
    else:
        in_specs = [cmap(cols[0]), cmap(cols[1]), cmap(cols[2]), cmap(OFF_LR // 128, 128),
                    pl.BlockSpec((128, HW), fixed), pl.BlockSpec((1, HW), fixed), st_spec, cmap(0)]
        args = (p, p, p, p, side[0], side[1], states, d_o)
        out_shape = [big, big, big, jax.ShapeDtypeStruct((rows, 128), F32), jax.ShapeDtypeStruct((128, HW), F32),
                     jax.ShapeDtypeStruct((8, HW), F32)]
        out_specs = [cmap(0), cmap(0), cmap(0), cmap(0, 128), pl.BlockSpec((128, HW), fixed),
                     pl.BlockSpec((8, HW), fixed)]
    return pl.pallas_call(
        body, name=name, grid=(n_chunks,), out_shape=out_shape, in_specs=in_specs, out_specs=out_specs,
        scratch_shapes=[pltpu.VMEM((NH, HD, HD), F32)],
        compiler_params=_cparams(dimension_semantics=("arbitrary",)),
    )(*args)


SMALL_ROWS = 56
ROWS_MOD_X = (0, 1, 8, 16, 17, 18)
ROWS_MOD_C = (2, 3)
ROW_PRE1, ROW_POST1, ROW_ONORM, ROW_PRE2, ROW_POST2, ROW_LB, ROW_BGK, ROW_WGK = 4, 9, 10, 19, 20, 24, 32, 40
ROW_LOSS = 21


def _reduce_small(gathered, lb_full, name):
    _, _, d = gathered.shape

    def body(g_ref, lb_ref, sum_ref, dmod_ref, dbmod_ref, dlb_ref):
        total = g_ref[0]
        for b in range(1, N_DEV):
            total = total + g_ref[b]
        sum_ref[...] = total
        dmod_ref[...] = jnp.zeros_like(dmod_ref)
        for m in range(N_MOD):
            col = slice(m * d, (m + 1) * d)
            acc = jnp.zeros((1, d), F32)
            for b in range(N_DEV):
                row = g_ref[b, ROWS_MOD_X[m]:ROWS_MOD_X[m] + 1, :]
                dmod_ref[b:b + 1, col] = row
                acc = acc + row
            if m < 2:
                ctx_row = total[ROWS_MOD_C[m]:ROWS_MOD_C[m] + 1, :]
                dmod_ref[8:9, col] = ctx_row
                acc = acc + ctx_row
            dbmod_ref[:, col] = acc
        lbv = lb_ref[...]
        for dd in range(2):
            lb = _sigmoid(lbv[dd:dd + 1, :] - lbv[2 + dd:3 + dd, :])
            gl = total[ROW_LB:ROW_LB + 1, dd * HW:(dd + 1) * HW] * lb * (1.0 - lb)
            dlb_ref[dd:dd + 1, :] = gl
            dlb_ref[2 + dd:3 + dd, :] = -gl

    return pl.pallas_call(
        body, name=name,
        out_shape=[jax.ShapeDtypeStruct((SMALL_ROWS, d), F32), jax.ShapeDtypeStruct((16, N_MOD * d), F32),
                   jax.ShapeDtypeStruct((1, N_MOD * d), F32), jax.ShapeDtypeStruct((4, HW), F32)],
        in_specs=[VMEM_SPEC] * 2, out_specs=[VMEM_SPEC] * 4, compiler_params=_cparams(),
    )(gathered, lb_full)


def _c_ctx_grad(gathered, c_ctx_row, name):
    def body(g_ref, c_ref, o_ref):
        acc = g_ref[0, 0:1, :]
        for chip in range(1, N_CHIP):
            acc = acc + g_ref[2 * chip, 0:1, :]
        o_ref[...] = acc * _dsilu(c_ref[...])

    return pl.pallas_call(
        body, name=name, out_shape=jax.ShapeDtypeStruct(c_ctx_row.shape, F32),
        in_specs=[VMEM_SPEC] * 2, out_specs=VMEM_SPEC, compiler_params=_cparams(),
    )(gathered, c_ctx_row)


def _relayout_w_in(w):
    pad = jnp.zeros((w.shape[0], 128 - 2 * RANK), w.dtype)
    return jnp.concatenate([w[:, :9 * HW], w[:, 9 * HW + 2 * RANK:], w[:, 9 * HW:9 * HW + 2 * RANK], pad], axis=1)


def _relayout_w_in_rows(wt):
    pad = jnp.zeros((128 - 2 * RANK, wt.shape[1]), wt.dtype)
    return jnp.concatenate([wt[:9 * HW], wt[9 * HW + 2 * RANK:], wt[9 * HW:9 * HW + 2 * RANK], pad], axis=0)


def _w_in_grad_rows(g_main, g_lr):
    return jnp.concatenate([g_main[:9 * HW], g_lr[:2 * RANK], g_main[9 * HW:]], axis=0)


def _w_in_grad_blocks(g_main, g_lr, n_blocks):
    lr0 = 9 * HW
    n = (g_main.shape[1] + 2 * RANK) // n_blocks

    def cols(lo, hi):
        out = []
        if lo < lr0:
            out.append(g_main[:, lo:min(hi, lr0)])
        if hi > lr0 and lo < lr0 + 2 * RANK:
            out.append(g_lr[:, max(lo, lr0) - lr0:min(hi, lr0 + 2 * RANK) - lr0])
        if hi > lr0 + 2 * RANK:
            out.append(g_main[:, max(lo, lr0 + 2 * RANK) - 2 * RANK:hi - 2 * RANK])
        return out

    return jnp.stack([jnp.concatenate(cols(j * n, (j + 1) * n), axis=1) for j in range(n_blocks)])


def _blocked(full, n_blocks):
    k, n = full.shape
    return full.reshape(k, n_blocks, n // n_blocks).transpose(1, 0, 2)


def _unblocked(blocks):
    nb, k, n = blocks.shape
    return blocks.transpose(1, 0, 2).reshape(k, nb * n)


def _sample_front(x0, ctx0, modc, modx, norm_pre1, lb_full, gla_side, w_in_r):
    ctx_len = ctx0.shape[0]
    n_ctx_tiles = ctx_len // TM
    n_ctx_chunks = ctx_len // CHUNK
    h1, p = _in_projection(ctx0, x0, modc, modx, norm_pre1, w_in_r, n_ctx_tiles, "in_projection")
    o_hg_fw, st_hg_fw, o_hg_bw, st_hg_bw = _scan_fwd_both(p, lb_full, n_ctx_chunks, "hg", "scan_hg")
    o_gla_fw, st_gla_fw, o_gla_bw, st_gla_bw = _scan_fwd_both(p, gla_side, n_ctx_chunks, "gla", "scan_gla")
    return dict(h1=h1, p=p, o_list=[o_hg_fw, o_hg_bw, o_gla_fw, o_gla_bw],
                states=[st_hg_fw, st_hg_bw, st_gla_fw, st_gla_bw])


def _sample_back(reduce, front, x0, ctx0, target0, modc, modx, norm_pre1, norms, onorms, lb_full, gla_side, w_in_r,
                 wbh, wbg, wout, wg, wu, wd):
    seq, d = x0.shape
    ctx_len = ctx0.shape[0]
    n_ctx_tiles = ctx_len // TM
    n_tiles = (ctx_len + seq) // TM
    n_ctx_chunks = ctx_len // CHUNK
    h1, p, o_list = front["h1"], front["p"], front["o_list"]
    st_hg_fw, st_hg_bw, st_gla_fw, st_gla_bw = front["states"]
    z2, y1, merged, og_hg, og_gla = _mixer_tail_fwd(x0, p, o_list, modx, norms, onorms, wbh, wbg, wout, n_ctx_tiles,
                                                    "mixer_tail")
    loss_part, dz2, h2, a_act, du, dv, dy2, stat_ffn = _ffn_fwd_bwd(z2, modx, norms, wg, wu, wd, target0, "ffn")
    dff = wg.shape[0]
    tok = reduce("ffn", [_weight_grad(du, h2, "grad_w_ff_gate", tk=dff // 2, tn=d),
                         _weight_grad(dv, h2, "grad_w_ff_up", tk=dff // 2, tn=d),
                         _weight_grad(a_act, dy2, "grad_w_ff_down", tk=dff // 2)])

    (d_ohg, d_ogla, d_hgate, d_ggate, d_ghg, d_ggla, dy1, db_hg, db_gla, stat_mix) = _mixer_tail_bwd(
        x0, p, o_list, dz2, y1, modx + tok, norms, onorms, wbh, wbg, wout, n_ctx_tiles, n_tiles, "mixer_tail_bwd")
    tok = reduce("mix", [_weight_grad(og_hg, db_hg, "grad_w_br_hg"), _weight_grad(og_gla, db_gla, "grad_w_br_gla"),
                         _weight_grad(merged, dy1, "grad_w_out")])
    tok = tok + reduce("push_ffn", [dy1])
    gla_b = [(wgk, bias + tok) for wgk, bias in gla_side]
    (dgq_f, dgk_f, dgv_f, dlr_f, dwgk_f, dbgk_f, dgq_b, dgk_b, dgv_b, dlr_b, dwgk_b, dbgk_b) = _scan_bwd_both(
        p, gla_b, (st_gla_fw, st_gla_bw), d_ogla, n_ctx_chunks, "gla", "scan_gla_bwd")
    lb_b = lb_full + reduce("push_mix", [dbgk_f])
    (dhq_f, dhi_f, dhf_f, dlb_f, dhq_b, dhi_b, dhf_b, dlb_b) = _scan_bwd_both(
        p, lb_b, (st_hg_fw, st_hg_bw), d_ohg, n_ctx_chunks, "hg", "scan_hg_bwd")
    pieces = [dhq_f, dhq_b, dhi_f, dhi_b, dhf_f, dhf_b, d_hgate, dgq_f, dgq_b, dgk_f, dgk_b, dgv_f, dgv_b, d_ggate,
              d_ghg, d_ggla, dlr_f, dlr_b]
    dp, grad_x, stat_in = _in_projection_bwd(ctx0, x0, dz2, modc, modx, norm_pre1, w_in_r, pieces, n_ctx_tiles,
                                             "in_projection_bwd")

    tok = reduce("in", [_w_in_grad(dp, h1, w_in_r.shape[0], "grad_w_in")])
    reduce("small", dict(stat_in=stat_in + tok, stat_mix=stat_mix, stat_ffn=stat_ffn, dlb=(dlb_f, dlb_b),
                         dwgk=(dwgk_f, dwgk_b), dbgk=(dbgk_f, dbgk_b)))
    reduce("push_in", [])
    return dict(
        loss_part=loss_part, grad_x=grad_x, stat_in=stat_in, stat_mix=stat_mix, stat_ffn=stat_ffn,
        dlb=(dlb_f, dlb_b), dwgk=(dwgk_f, dwgk_b), dbgk=(dbgk_f, dbgk_b))


def kernel(x, c, ctx, c_ctx, w_mod, b_mod, norm_pre1, norm_post1, norm_pre2, norm_post2, w_in, hg_lb, hg_onorm, gla_w_gk, gla_b_gk, gla_onorm, w_br_hg, w_br_gla, w_out, w_ff_gate, w_ff_up, w_ff_down, loss_target, m_c_ctx, m_w_mod, m_b_mod, m_norm_pre1, m_norm_post1, m_norm_pre2, m_norm_post2, m_w_in, m_hg_lb, m_hg_onorm, m_gla_w_gk, m_gla_b_gk, m_gla_onorm, m_w_br_hg, m_w_br_gla, m_w_out, m_w_ff_gate, m_w_ff_up, m_w_ff_down, v_c_ctx, v_w_mod, v_b_mod, v_norm_pre1, v_norm_post1, v_norm_pre2, v_norm_post2, v_w_in, v_hg_lb, v_hg_onorm, v_gla_w_gk, v_gla_b_gk, v_gla_onorm, v_w_br_hg, v_w_br_gla, v_w_out, v_w_ff_gate, v_w_ff_up, v_w_ff_down):
    seq, d = x.shape[1], x.shape[2]
    ctx_len = ctx.shape[1]
    assert seq % TM == 0 and ctx_len % TM == 0 and d == 2 * HW
    ax, ay, ac = lax.axis_index("x"), lax.axis_index("y"), lax.axis_index("c")
    chip = 2 * ax + ay
    dev = 2 * chip + ac
    c_arr = jnp.reshape(ac, (1,)).astype(jnp.int32)
    chip_arr = jnp.reshape(chip, (1,)).astype(jnp.int32)
    transposed = ("w_in", "w_ff_gate", "w_ff_up")
    view = lambda a, nm: a[0].T if nm in transposed else a[0]

    sems_in, lands_in, token_in0 = _blocks_start([_cast_into_blocks(chip_arr, view(w_in, "w_in"), "cast_w_in")],
                                                 "gather_w_in_start")

    nc = d // 128
    pad8 = lambda a: jnp.pad(a, ((0, -a.shape[0] % 8), (0, 0)))
    small1 = jnp.concatenate([c.reshape(nc, 128) + token_in0[0, 0], pad8(hg_lb.reshape(4, 128)),
                              gla_w_gk.reshape(2 * RANK, 128), pad8(gla_b_gk.reshape(2, 128))], axis=0)
    blocks = [_cast_into_blocks(chip_arr, view(w_, nm), "cast_" + nm) for w_, nm in (
        (w_br_hg, "w_br_hg"), (w_br_gla, "w_br_gla"), (w_out, "w_out"), (w_ff_gate, "w_ff_gate"),
        (w_ff_up, "w_ff_up"), (w_ff_down, "w_ff_down"))]
    got1 = _allgather8(small1, "gather_small_params", after=blocks)
    c_all = got1[:, :nc, :].reshape(N_DEV, d)
    per_chip = got1[0::2]
    lb_full = per_chip[:, nc:nc + 4, :].transpose(1, 0, 2).reshape(4, HW)
    wgk_full = per_chip[:, nc + 8:nc + 8 + 2 * RANK, :].transpose(1, 0, 2).reshape(2, RANK, HW)
    bgk_full = per_chip[:, nc + 8 + 2 * RANK:nc + 10 + 2 * RANK, :].transpose(1, 0, 2).reshape(2, HW)
    wgk_pad = [jnp.zeros((128, HW), F32).at[dd * RANK:(dd + 1) * RANK].set(wgk_full[dd]) for dd in range(2)]
    bgk = [bgk_full[dd:dd + 1] for dd in range(2)]

    n_mod_cols = w_mod.shape[2]
    cond = jnp.concatenate([c_all, pad8(c_ctx.reshape(1, d))], axis=0)
    b_cols = lax.dynamic_slice(b_mod, (0, chip * n_mod_cols), (1, n_mod_cols))
    mod_part = _mod_forward(cond, w_mod[0], b_cols, "mod_forward")
    mod_got = _allgather8(mod_part, "gather_mod")
    mod_all = mod_got[0::2].transpose(1, 0, 2).reshape(16, N_CHIP * n_mod_cols)
    modx = pad8(lax.dynamic_slice(mod_all, (dev, 0), (1, N_MOD * d)).reshape(N_MOD, d))
    modc = pad8(mod_all[8].reshape(N_MOD, d))

    lands_in = _blocks_wait(sems_in, lands_in, [mod_got], "gather_w_in_wait")
    gathered_in = _blocks_finish(lands_in, "gather_w_in_finish")
    sems, lands, token = _blocks_start(blocks, "gather_rest_start", after=[gathered_in[0]])
    w_in_r = gathered_in[0].reshape(-1, d)

    norms = jnp.concatenate([norm_pre1, norm_post1, norm_pre2, norm_post2, jnp.zeros((4, d), F32)], axis=0)
    onorms = jnp.zeros((8, d), F32).at[0, :HD].set(hg_onorm[0]).at[1, :HD].set(gla_onorm[0])
    gla_side = [(wgk_pad[dd], bgk[dd]) for dd in range(2)]
    modx = modx + token[0, 0]
    front = _sample_front(x[0], ctx[0], modc, modx, norm_pre1, lb_full, gla_side, w_in_r)
    lands = _blocks_wait(sems, lands, front["o_list"], "gather_rest_wait")
    gathered = _blocks_finish(lands, "gather_rest_finish")
    wbh, wbg = _unblocked(gathered[0]), _unblocked(gathered[1])
    wout = gathered[2].reshape(d, d)
    wg, wu, wd = (gathered[i].reshape(-1, d) for i in (3, 4, 5))
    dff = wg.shape[0]
    groups = {"ffn": ["w_ff_gate", "w_ff_up", "w_ff_down"], "mix": ["w_br_hg", "w_br_gla", "w_out"], "in": ["w_in"]}
    row_sharded = {"w_out": d // N_CHIP, "w_ff_down": dff // N_CHIP, "w_ff_gate": dff // N_CHIP,
                   "w_ff_up": dff // N_CHIP, "w_in": w_in.shape[2]}
    in_flight, to_sibling = {}, {}

    small = {}

    def reduce_small(stats):
        small2 = jnp.concatenate([
            stats["stat_in"], stats["stat_mix"], stats["stat_ffn"],
            jnp.concatenate(stats["dlb"], axis=1), jnp.concatenate(stats["dbgk"], axis=1),
            jnp.concatenate([stats["dwgk"][0][0:RANK], stats["dwgk"][1][RANK:2 * RANK]], axis=1)], axis=0)
        assert small2.shape[0] == SMALL_ROWS
        got2 = _allgather8(small2, "gather_small_grads")
        total, dmod_all, g_b_mod, g_lb_full = _reduce_small(got2, lb_full, "reduce_small")
        dmod_cols = lax.dynamic_slice(dmod_all, (0, chip * n_mod_cols), (16, n_mod_cols))
        g_w_mod, cctx_part = _mod_backward(cond, w_mod[0], dmod_cols, "mod_backward")
        got3 = _allgather8(cctx_part, "gather_c_ctx_grad")
        g_c_ctx = _c_ctx_grad(got3, c_ctx.reshape(1, d), "c_ctx_grad")
        small.update(total=total, g_b_mod=g_b_mod, g_lb_full=g_lb_full, g_w_mod=g_w_mod, g_c_ctx=g_c_ctx)

    def reduce(group, grads):
        if group == "small":
            return reduce_small(grads)
        if group.startswith("push_"):
            return push(group[5:], grads)
        nms = groups[group]
        full = [g.reshape(N_CHIP, row_sharded[nm], d) if nm in row_sharded else _blocked(g, N_CHIP)
                for g, nm in zip(grads, nms)]
        sems_, full, lands_, token_ = _send_half_start(full, "grads_to_sibling_start_" + group)
        to_sibling[group] = (sems_, full, lands_)
        return token_[0, 0]

    def push(group, after):
        nms = groups[group]
        sems_, full, lands_ = to_sibling[group]
        if group == "in":
            after = list(after) + [small["g_c_ctx"], small["total"]]
        full, from_sibling = _send_half_wait(sems_, full, lands_, after, "grads_to_sibling_wait_" + group)
        pairs = [_pair_sum(c_arr, f, r_, "pair_sum_" + nm) for f, r_, nm in zip(full, from_sibling, nms)]
        after = [small["g_c_ctx"], small["total"]] if group == "in" else []
        sems_, pairs, lands_, token_ = _scatter_start(pairs, "grads_to_owner_start_" + group, after)
        in_flight[group] = (sems_, pairs, lands_, token_)
        return token_[0, 0]

    r = _sample_back(reduce, front, x[0], ctx[0], loss_target[0], modc, modx, norm_pre1, norms, onorms, lb_full,
                     gla_side, w_in_r, wbh, wbg, wout, wg, wu, wd)
    loss_part, grad_x, stat_in, stat_mix, stat_ffn = (r[k] for k in ("loss_part", "grad_x", "stat_in", "stat_mix",
                                                                     "stat_ffn"))
    (dlb_f, dlb_b), (dwgk_f, dwgk_b), (dbgk_f, dbgk_b) = r["dlb"], r["dwgk"], r["dbgk"]

    weights = dict(w_in=(w_in, m_w_in, v_w_in), w_br_hg=(w_br_hg, m_w_br_hg, v_w_br_hg),
                   w_br_gla=(w_br_gla, m_w_br_gla, v_w_br_gla), w_out=(w_out, m_w_out, v_w_out),
                   w_ff_gate=(w_ff_gate, m_w_ff_gate, v_w_ff_gate), w_ff_up=(w_ff_up, m_w_ff_up, v_w_ff_up),
                   w_ff_down=(w_ff_down, m_w_ff_down, v_w_ff_down))
    names = ["w_in", "w_br_hg", "w_br_gla", "w_out", "w_ff_gate", "w_ff_up", "w_ff_down"]
    big = {}

    swapping = {}

    def sum_and_swap(group, after):
        sems_, pairs, lands_, _ = in_flight[group]
        pairs, lands_ = _scatter_wait(sems_, pairs, lands_, after, "grads_to_owner_wait_" + group)
        own_half = [_sum_owner(chip_arr, pr, g, "chip_sum_" + nm) for pr, g, nm in zip(pairs, lands_, groups[group])]
        swapping[group] = _swap_start(own_half, "halves_to_sibling_start_" + group)
        return own_half[-1]

    def update(group, after):
        sems_, own_half, lands_ = swapping[group]
        own_half, other_half = _swap_wait(sems_, own_half, lands_, after, "halves_to_sibling_wait_" + group)
        for nm, own, oth in zip(groups[group], own_half, other_half):
            w_, m_, v_ = (view(a, nm) for a in weights[nm])
            res = _adamw_halves(c_arr, own, oth, w_, m_, v_, "adamw_" + nm)
            big[nm] = [r_.T[None] if nm in transposed else r_[None] for r_ in res]
        return big[groups[group][-1]][1]

    token_in = in_flight["in"][3]
    summed_ffn = sum_and_swap("ffn", [token_in])
    summed_mix = sum_and_swap("mix", [summed_ffn])

    total, g_b_mod, g_lb_full, g_w_mod, g_c_ctx = (small[k] for k in ("total", "g_b_mod", "g_lb_full", "g_w_mod",
                                                                      "g_c_ctx"))
    g_pre1, g_post1, g_pre2, g_post2 = (total[r_:r_ + 1] for r_ in (ROW_PRE1, ROW_POST1, ROW_PRE2, ROW_POST2))
    g_hg_on, g_gla_on = total[ROW_ONORM:ROW_ONORM + 1, 0:HD], total[ROW_ONORM:ROW_ONORM + 1, HD:2 * HD]
    n_lb = hg_lb.shape[2]
    g_hg_lb = lax.dynamic_slice(g_lb_full, (0, chip * n_lb), (4, n_lb))
    g_bgk = lax.dynamic_slice(total[ROW_BGK:ROW_BGK + 1].reshape(2, HW), (0, chip * n_lb), (2, n_lb))
    g_wgk_full = total[ROW_WGK:ROW_WGK + RANK].reshape(RANK, 2, HW).transpose(1, 0, 2).reshape(2 * RANK, HW)
    g_wgk = lax.dynamic_slice(g_wgk_full, (0, chip * n_lb), (2 * RANK, n_lb))

    small_items = [
        (g_c_ctx, c_ctx.reshape(1, d), m_c_ctx.reshape(1, d), v_c_ctx.reshape(1, d)),
        (g_b_mod, b_mod, m_b_mod, v_b_mod),
        (g_pre1, norm_pre1, m_norm_pre1, v_norm_pre1),
        (g_post1, norm_post1, m_norm_post1, v_norm_post1),
        (g_pre2, norm_pre2, m_norm_pre2, v_norm_pre2),
        (g_post2, norm_post2, m_norm_post2, v_norm_post2),
        (g_hg_lb, hg_lb.reshape(4, n_lb), m_hg_lb.reshape(4, n_lb), v_hg_lb.reshape(4, n_lb)),
        (g_hg_on, hg_onorm, m_hg_onorm, v_hg_onorm),
        (g_wgk, gla_w_gk.reshape(2 * RANK, n_lb), m_gla_w_gk.reshape(2 * RANK, n_lb), v_gla_w_gk.reshape(2 * RANK, n_lb)),
        (g_bgk, gla_b_gk.reshape(2, n_lb), m_gla_b_gk.reshape(2, n_lb), v_gla_b_gk.reshape(2, n_lb)),
        (g_gla_on, gla_onorm, m_gla_onorm, v_gla_onorm),
    ]
    small_res = _adamw_whole(small_items, "adamw_small")
    mod_res = _adamw_tiled(g_w_mod, w_mod[0], m_w_mod[0], v_w_mod[0], "adamw_w_mod")
    done_ffn = update("ffn", [summed_mix, mod_res[0], small_res[0][0]])
    done_mix = update("mix", [done_ffn])
    update("in", [sum_and_swap("in", [done_mix])])

    loss = total[ROW_LOSS, 0]

    shapes = dict(c_ctx=c_ctx.shape, b_mod=b_mod.shape, norm_pre1=norm_pre1.shape, norm_post1=norm_post1.shape,
                  norm_pre2=norm_pre2.shape, norm_post2=norm_post2.shape, hg_lb=hg_lb.shape, hg_onorm=hg_onorm.shape,
                  gla_w_gk=gla_w_gk.shape, gla_b_gk=gla_b_gk.shape, gla_onorm=gla_onorm.shape)
    small_names = ["c_ctx", "b_mod", "norm_pre1", "norm_post1", "norm_pre2", "norm_post2", "hg_lb", "hg_onorm",
                   "gla_w_gk", "gla_b_gk", "gla_onorm"]
    grads, deltas, new_m, new_v = {}, {}, {}, {}
    for nm, item, res in zip(small_names, small_items, small_res):
        grads[nm] = item[0].reshape(shapes[nm])
        deltas[nm], new_m[nm], new_v[nm] = (r.reshape(shapes[nm]) for r in res)
    grads["w_mod"] = g_w_mod[None]
    deltas["w_mod"], new_m["w_mod"], new_v["w_mod"] = (r[None] for r in mod_res)
    for nm in names:
        grads[nm], deltas[nm], new_m[nm], new_v[nm] = big[nm]
    order = ["c_ctx", "w_mod", "b_mod", "norm_pre1", "norm_post1", "norm_pre2", "norm_post2", "w_in", "hg_lb",
             "hg_onorm", "gla_w_gk", "gla_b_gk", "gla_onorm", "w_br_hg", "w_br_gla", "w_out", "w_ff_gate", "w_ff_up",
             "w_ff_down"]
    return (loss, grad_x[None], *[grads[n] for n in order], *[deltas[n] for n in order],
            *[new_m[n] for n in order], *[new_v[n] for n in order])


def _weight_grad_cols(xs, dy, n_cols, name, tn=512):
    rows = dy.shape[0]
    k = tk = xs.shape[1]

    return pl.pallas_call(
        functools.partial(_transposed_lhs_matmul), name=name, grid=(k // tk, n_cols // tn),
        out_shape=jax.ShapeDtypeStruct((k, n_cols), F32),
        in_specs=[pl.BlockSpec((rows, tk), lambda i, j: (0, i)), pl.BlockSpec((rows, tn), lambda i, j: (0, j))],
        out_specs=pl.BlockSpec((tk, tn), lambda i, j: (i, j)),
        scratch_shapes=[pltpu.VMEM((tk, rows), BF16)],
        compiler_params=_cparams(dimension_semantics=("parallel", "arbitrary")),
    )(xs, dy)
```

```python
import functools

import jax
import jax.numpy as jnp
from jax import lax
from jax.experimental import pallas as pl
from jax.experimental.pallas import tpu as pltpu

F32 = jnp.float32
BF16 = jnp.bfloat16
HIGHEST = lax.Precision.HIGHEST
MESH = pl.DeviceIdType.MESH

EPS = 1e-6
CHUNK = 64
SUB = 16
NSUB = CHUNK // SUB
NH = 4
HD = 128
HW = NH * HD
RANK = 16
GATE_NORM = 16.0
N_MOD = 6
TM = 256
TM_FFN = 128
N_DEV = 8
N_CHIP = 4
VMEM_LIMIT = 56 * 1024 * 1024

ADAM_LR = 0.001
ADAM_B1 = 0.9
ADAM_B2 = 0.999
ADAM_EPS = 1e-08
ADAM_WD = 0.01
ADAM_STEP = 10

VMEM_SPEC = pl.BlockSpec(memory_space=pltpu.VMEM)
ANY_SPEC = pl.BlockSpec(memory_space=pl.ANY)
HBM_SPEC = pl.BlockSpec(memory_space=pltpu.HBM)
SEM_SPEC = pl.BlockSpec(memory_space=pltpu.SEMAPHORE)
EFFECT = pltpu.SideEffectType.DATAFLOW_SIDE_EFFECTING


def _cparams(**kw):
    return pltpu.CompilerParams(vmem_limit_bytes=VMEM_LIMIT, **kw)


def _dot(a, b):
    return jnp.dot(a.astype(BF16), b.astype(BF16), preferred_element_type=F32)


def _dot_nt(a, b):
    return lax.dot_general(a.astype(BF16), b.astype(BF16), (((1,), (1,)), ((), ())), preferred_element_type=F32)


def _dot_tn(a, b):
    return lax.dot_general(a.astype(BF16), b.astype(BF16), (((0,), (0,)), ((), ())), preferred_element_type=F32)


def _sigmoid(x):
    return 1.0 / (1.0 + jnp.exp(-x))


def _silu(x):
    return x * _sigmoid(x)


def _dsilu(x):
    s = _sigmoid(x)
    return s * (1.0 + x * (1.0 - s))


def _log_sigmoid(x):
    return jnp.minimum(x, 0.0) - jnp.log(1.0 + jnp.exp(-jnp.abs(x)))


def _colsum(a):
    return jnp.sum(a, axis=0, keepdims=True)


def _rms(a):
    r = lax.rsqrt(jnp.mean(a * a, axis=-1, keepdims=True) + EPS)
    return a * r, r


def _rms_bwd(dn, n, r):
    return r * (dn - n * jnp.mean(dn * n, axis=-1, keepdims=True))


def _place():
    x, y, c = lax.axis_index("x"), lax.axis_index("y"), lax.axis_index("c")
    chips = [(1 - x, y), (x, 1 - y), (1 - x, 1 - y)]
    return x, y, c, chips


def _allgather8(v, name, after=()):
    rows, cols = v.shape
    n_after = len(after)

    def body(x_ref, *rest):
        out_ref, send_sems, recv_sems, local_sem = rest[n_after:]
        x, y, c, chips = _place()
        me, sibling = (x, y, c), (x, y, 1 - c)

        def blk(px, py, pc):
            return out_ref.at[4 * px + 2 * py + pc]

        def copy(k, block, to, src=None):
            return pltpu.make_async_remote_copy(
                src_ref=blk(*block) if src is None else src, dst_ref=blk(*block),
                send_sem=send_sems.at[k], recv_sem=recv_sems.at[k], device_id=to, device_id_type=MESH)

        mine = pltpu.make_async_copy(x_ref, blk(*me), local_sem)
        mine.start()
        first = [copy(0, me, sibling, src=x_ref)]
        first += [copy(1 + j, me, (*chip, c), src=x_ref) for j, chip in enumerate(chips)]
        for cp in first:
            cp.start()
        passed = [copy(4 + j, (*chip, c), sibling) for j, chip in enumerate(chips)]
        for j, chip in enumerate(chips):
            copy(1 + j, (*chip, c), me).wait_recv()
            passed[j].start()
        copy(0, sibling, me).wait_recv()
        for j, chip in enumerate(chips):
            copy(4 + j, (*chip, 1 - c), me).wait_recv()
        for cp in first + passed:
            cp.wait_send()
        mine.wait()

    return pl.pallas_call(
        body, name=name,
        out_shape=jax.ShapeDtypeStruct((N_DEV, rows, cols), v.dtype),
        in_specs=[VMEM_SPEC] + [ANY_SPEC] * n_after, out_specs=VMEM_SPEC,
        scratch_shapes=[pltpu.SemaphoreType.DMA((7,)), pltpu.SemaphoreType.DMA((7,)), pltpu.SemaphoreType.DMA],
    )(v, *after)


def _cast_into_blocks(chip_arr, w, name):
    rows, cols = w.shape
    tr = _row_tile(rows, 16, 256)

    def body(chip_ref, w_ref, o_ref):
        o_ref[0] = w_ref[...].astype(BF16)

    return pl.pallas_call(
        body, name=name,
        grid_spec=pltpu.PrefetchScalarGridSpec(
            num_scalar_prefetch=1, grid=(rows // tr,),
            in_specs=[pl.BlockSpec((tr, cols), lambda i, chip_ref: (i, 0))],
            out_specs=pl.BlockSpec((1, tr, cols), lambda i, chip_ref: (chip_ref[0], i, 0))),
        out_shape=jax.ShapeDtypeStruct((N_CHIP, rows, cols), BF16),
        compiler_params=_cparams(dimension_semantics=("parallel",)),
    )(chip_arr, w)


def _halved_by_rows(shape):
    return (shape[1] // 2) % 16 == 0


def _half_of(ref, pc, block=None):
    lead = slice(None) if block is None else block
    if _halved_by_rows(ref.shape):
        h = ref.shape[1] // 2
        return ref.at[lead, pl.ds(pl.multiple_of(pc * h, 16), h), :]
    h = ref.shape[2] // 2
    return ref.at[lead, :, pl.ds(pl.multiple_of(pc * h, 128), h)]


def _half_shape(shape):
    return (shape[0], shape[1] // 2, shape[2]) if _halved_by_rows(shape) else (shape[0], shape[1], shape[2] // 2)


def _half_rows(ref, chip_id, pc):
    return _half_of(ref, pc, chip_id)


def _gather_blocks(lands, name, after=()):
    n = len(lands)
    n_in = n + len(after)

    def body(*refs):
        outs = refs[n_in:n_in + n]
        send_sems, recv_sems = refs[n_in + n:]
        x, y, c, chips = _place()
        me_chip = 2 * x + y
        sibling = (x, y, 1 - c)

        def copy(k, j, chip_id, pc, to):
            return pltpu.make_async_remote_copy(
                src_ref=_half_rows(outs[k], chip_id, pc), dst_ref=_half_rows(outs[k], chip_id, pc),
                send_sem=send_sems.at[k, j], recv_sem=recv_sems.at[k, j], device_id=to, device_id_type=MESH)

        started = []
        for k in range(n):
            for j, chip in enumerate(chips):
                cp = copy(k, j, me_chip, c, (*chip, c))
                cp.start()
                started.append(cp)
        for k in range(n):
            for j, (px, py) in enumerate(chips):
                copy(k, j, 2 * px + py, c, sibling).wait_recv()
                cp = copy(k, 3 + j, 2 * px + py, c, sibling)
                cp.start()
                started.append(cp)
        for k in range(n):
            for j, (px, py) in enumerate(chips):
                copy(k, 3 + j, 2 * px + py, 1 - c, sibling).wait_recv()
        for cp in started:
            cp.wait_send()

    return pl.pallas_call(
        body, name=name,
        out_shape=[jax.ShapeDtypeStruct(l.shape, l.dtype) for l in lands],
        in_specs=[ANY_SPEC] * n_in, out_specs=[ANY_SPEC] * n,
        input_output_aliases={i: i for i in range(n)},
        scratch_shapes=[pltpu.SemaphoreType.DMA((n, 6)), pltpu.SemaphoreType.DMA((n, 6))],
    )(*lands, *after)


def _hbm(a):
    return pltpu.with_memory_space_constraint(a, pltpu.HBM)


def _blocks_start(lands, name, after=()):
    n = len(lands)
    n_sem = 3 * n
    first = n + len(after)

    def body(*refs):
        lnd = refs[:n]
        send_sems, recv_sems = refs[first:first + n_sem], refs[first + n_sem:first + 2 * n_sem]
        token = refs[-1]
        x, y, c, chips = _place()
        me_chip = 2 * x + y
        for k in range(n):
            for j, chip in enumerate(chips):
                pltpu.make_async_remote_copy(
                    src_ref=_half_rows(lnd[k], me_chip, c), dst_ref=_half_rows(lnd[k], me_chip, c),
                    send_sem=send_sems[3 * k + j], recv_sem=recv_sems[3 * k + j],
                    device_id=(*chip, c), device_id_type=MESH).start()
        token[...] = jnp.zeros_like(token)

    out = pl.pallas_call(
        body, name=name,
        out_shape=(*[pltpu.SemaphoreType.DMA(())] * (2 * n_sem),
                   *[pltpu.HBM(l.shape, l.dtype) for l in lands],
                   jax.ShapeDtypeStruct((8, 128), F32)),
        in_specs=[HBM_SPEC] * n + [ANY_SPEC] * len(after),
        out_specs=(*[SEM_SPEC] * (2 * n_sem), *[HBM_SPEC] * n, VMEM_SPEC),
        input_output_aliases={i: 2 * n_sem + i for i in range(n)},
        compiler_params=pltpu.CompilerParams(has_side_effects=EFFECT),
    )(*[_hbm(l) for l in lands], *after)
    return list(out[:2 * n_sem]), list(out[2 * n_sem:2 * n_sem + n]), out[-1]


def _blocks_wait(sems, lands, after, name):
    n = len(lands)
    n_sem = 3 * n

    def body(*refs):
        lnd = refs[:n]
        s_sems, r_sems = refs[n:n + n_sem], refs[n + n_sem:n + 2 * n_sem]
        x, y, c, chips = _place()
        me_chip = 2 * x + y
        for k in range(n):
            for j, (px, py) in enumerate(chips):
                cp = pltpu.make_async_remote_copy(
                    src_ref=_half_rows(lnd[k], me_chip, c), dst_ref=_half_rows(lnd[k], 2 * px + py, c),
                    send_sem=s_sems[3 * k + j], recv_sem=r_sems[3 * k + j],
                    device_id=(px, py, c), device_id_type=MESH)
                cp.wait_send()
                cp.wait_recv()

    out = pl.pallas_call(
        body, name=name,
        out_shape=tuple(pltpu.HBM(l.shape, l.dtype) for l in lands),
        in_specs=[HBM_SPEC] * n + [SEM_SPEC] * (2 * n_sem) + [ANY_SPEC] * len(after),
        out_specs=[HBM_SPEC] * n,
        input_output_aliases={i: i for i in range(n)},
        compiler_params=pltpu.CompilerParams(has_side_effects=EFFECT),
    )(*lands, *sems, *after)
    return list(out)


def _forward_start(lands, name):
    n = len(lands)
    n_sem = 3 * n

    def body(*refs):
        lnd = refs[:n]
        send_sems, recv_sems = refs[n:n + n_sem], refs[n + n_sem:n + 2 * n_sem]
        x, y, c, chips = _place()
        for k in range(n):
            for j, (px, py) in enumerate(chips):
                pltpu.make_async_remote_copy(
                    src_ref=_half_rows(lnd[k], 2 * px + py, c), dst_ref=_half_rows(lnd[k], 2 * px + py, c),
                    send_sem=send_sems[3 * k + j], recv_sem=recv_sems[3 * k + j],
                    device_id=(x, y, 1 - c), device_id_type=MESH).start()
        refs[-1][...] = jnp.zeros_like(refs[-1])

    out = pl.pallas_call(
        body, name=name,
        out_shape=(*[pltpu.SemaphoreType.DMA(())] * (2 * n_sem), *[pltpu.HBM(l.shape, l.dtype) for l in lands],
                   jax.ShapeDtypeStruct((8, 128), F32)),
        in_specs=[HBM_SPEC] * n,
        out_specs=(*[SEM_SPEC] * (2 * n_sem), *[HBM_SPEC] * n, VMEM_SPEC),
        input_output_aliases={i: 2 * n_sem + i for i in range(n)},
        compiler_params=pltpu.CompilerParams(has_side_effects=EFFECT),
    )(*[_hbm(l) for l in lands])
    return list(out[:2 * n_sem]), list(out[2 * n_sem:2 * n_sem + n]), out[-1]


def _forward_wait(sems, lands, after, name):
    n = len(lands)
    n_sem = 3 * n

    def body(*refs):
        lnd = refs[:n]
        s_sems, r_sems = refs[n:n + n_sem], refs[n + n_sem:n + 2 * n_sem]
        x, y, c, chips = _place()
        for k in range(n):
            for j, (px, py) in enumerate(chips):
                cp = pltpu.make_async_remote_copy(
                    src_ref=_half_rows(lnd[k], 2 * px + py, c), dst_ref=_half_rows(lnd[k], 2 * px + py, 1 - c),
                    send_sem=s_sems[3 * k + j], recv_sem=r_sems[3 * k + j],
                    device_id=(x, y, 1 - c), device_id_type=MESH)
                cp.wait_send()
                cp.wait_recv()

    out = pl.pallas_call(
        body, name=name,
        out_shape=tuple(pltpu.HBM(l.shape, l.dtype) for l in lands),
        in_specs=[HBM_SPEC] * n + [SEM_SPEC] * (2 * n_sem) + [ANY_SPEC] * len(after),
        out_specs=[HBM_SPEC] * n,
        input_output_aliases={i: i for i in range(n)},
        compiler_params=pltpu.CompilerParams(has_side_effects=EFFECT),
    )(*lands, *sems, *after)
    return list(out)


def _blocks_finish(lands, name):
    n = len(lands)

    def body(*refs):
        lnd = refs[n:2 * n]
        send_sems, recv_sems = refs[2 * n:]
        x, y, c, chips = _place()
        sibling = (x, y, 1 - c)

        def copy(k, j, chip_id, pc):
            return pltpu.make_async_remote_copy(
                src_ref=_half_rows(lnd[k], chip_id, pc), dst_ref=_half_rows(lnd[k], chip_id, pc),
                send_sem=send_sems.at[k, j], recv_sem=recv_sems.at[k, j], device_id=sibling, device_id_type=MESH)

        started = []
        for k in range(n):
            for j, (px, py) in enumerate(chips):
                cp = copy(k, j, 2 * px + py, c)
                cp.start()
                started.append(cp)
        for k in range(n):
            for j, (px, py) in enumerate(chips):
                copy(k, j, 2 * px + py, 1 - c).wait_recv()
        for cp in started:
            cp.wait_send()

    out = pl.pallas_call(
        body, name=name,
        out_shape=[jax.ShapeDtypeStruct(l.shape, l.dtype) for l in lands],
        in_specs=[ANY_SPEC] * n, out_specs=[ANY_SPEC] * n,
        input_output_aliases={i: i for i in range(n)},
        scratch_shapes=[pltpu.SemaphoreType.DMA((n, 3)), pltpu.SemaphoreType.DMA((n, 3))],
    )(*lands)
    return list(out)


def _gather_start(shards, name):
    n = len(shards)
    n_sem = 3 * n

    def body(*refs):
        ins, lands = refs[:n], refs[n:2 * n]
        send_sems, recv_sems = refs[2 * n:2 * n + n_sem], refs[2 * n + n_sem:2 * n + 2 * n_sem]
        token = refs[-1]
        x, y, c, chips = _place()
        me_chip = 2 * x + y
        for k in range(n):
            h = shards[k].shape[0] // 2
            rows = pl.ds(pl.multiple_of(c * h, 8), h)
            for j, chip in enumerate(chips):
                pltpu.make_async_remote_copy(
                    src_ref=ins[k].at[rows, :], dst_ref=lands[k].at[me_chip, rows, :],
                    send_sem=send_sems[3 * k + j], recv_sem=recv_sems[3 * k + j],
                    device_id=(*chip, c), device_id_type=MESH).start()
        token[...] = jnp.zeros_like(token)

    lands = [_hbm(lax.empty((N_CHIP,) + s.shape, s.dtype)) for s in shards]
    out = pl.pallas_call(
        body, name=name,
        out_shape=(*[pltpu.SemaphoreType.DMA(())] * (2 * n_sem),
                   *[pltpu.HBM(s.shape, s.dtype) for s in shards],
                   *[pltpu.HBM(l.shape, l.dtype) for l in lands],
                   jax.ShapeDtypeStruct((8, 128), F32)),
        in_specs=[HBM_SPEC] * (2 * n),
        out_specs=(*[SEM_SPEC] * (2 * n_sem), *[HBM_SPEC] * (2 * n), VMEM_SPEC),
        input_output_aliases={i: 2 * n_sem + i for i in range(2 * n)},
        compiler_params=pltpu.CompilerParams(has_side_effects=EFFECT),
    )(*[_hbm(s) for s in shards], *lands)
    sems = list(out[:2 * n_sem])
    return sems, list(out[2 * n_sem:2 * n_sem + n]), list(out[2 * n_sem + n:2 * n_sem + 2 * n]), out[-1]


def _gather_wait(sems, shards, lands, after, name):
    n = len(shards)
    n_sem = 3 * n

    def body(*refs):
        ins, lnd = refs[:n], refs[n:2 * n]
        s_sems, r_sems = refs[2 * n:2 * n + n_sem], refs[2 * n + n_sem:2 * n + 2 * n_sem]
        x, y, c, chips = _place()
        for k in range(n):
            h = shards[k].shape[0] // 2
            rows = pl.ds(pl.multiple_of(c * h, 8), h)
            for j, (px, py) in enumerate(chips):
                cp = pltpu.make_async_remote_copy(
                    src_ref=ins[k].at[rows, :], dst_ref=lnd[k].at[2 * px + py, rows, :],
                    send_sem=s_sems[3 * k + j], recv_sem=r_sems[3 * k + j],
                    device_id=(px, py, c), device_id_type=MESH)
                cp.wait_send()
                cp.wait_recv()

    out = pl.pallas_call(
        body, name=name,
        out_shape=(*[pltpu.HBM(s.shape, s.dtype) for s in shards], *[pltpu.HBM(l.shape, l.dtype) for l in lands]),
        in_specs=[HBM_SPEC] * (2 * n) + [SEM_SPEC] * (2 * n_sem) + [ANY_SPEC],
        out_specs=[HBM_SPEC] * (2 * n),
        input_output_aliases={i: i for i in range(2 * n)},
        compiler_params=pltpu.CompilerParams(has_side_effects=EFFECT),
    )(*shards, *lands, *sems, after)
    return list(out[:n]), list(out[n:])


def _gather_finish(shards, lands, name):
    n = len(shards)

    def body(*refs):
        ins, lnd = refs[:n], refs[2 * n:3 * n]
        send_sems, recv_sems, local_sems = refs[3 * n:]
        x, y, c, chips = _place()
        me_chip = 2 * x + y
        sibling = (x, y, 1 - c)

        def half(k, chip_id, pc):
            h = shards[k].shape[0] // 2
            return lnd[k].at[chip_id, pl.ds(pl.multiple_of(pc * h, 8), h), :]

        def copy(k, j, chip_id, pc):
            return pltpu.make_async_remote_copy(
                src_ref=half(k, chip_id, pc), dst_ref=half(k, chip_id, pc),
                send_sem=send_sems.at[k, j], recv_sem=recv_sems.at[k, j], device_id=sibling, device_id_type=MESH)

        locals_, started = [], []
        for k in range(n):
            cp = pltpu.make_async_copy(ins[k], lnd[k].at[me_chip], local_sems.at[k])
            cp.start()
            locals_.append(cp)
            for j, (px, py) in enumerate(chips):
                cp = copy(k, j, 2 * px + py, c)
                cp.start()
                started.append(cp)
        for k in range(n):
            for j, (px, py) in enumerate(chips):
                copy(k, j, 2 * px + py, 1 - c).wait_recv()
        for cp in started:
            cp.wait_send()
        for cp in locals_:
            cp.wait()

    out = pl.pallas_call(
        body, name=name,
        out_shape=[jax.ShapeDtypeStruct(l.shape, l.dtype) for l in lands],
        in_specs=[ANY_SPEC] * (2 * n), out_specs=[ANY_SPEC] * n,
        input_output_aliases={n + i: i for i in range(n)},
        scratch_shapes=[pltpu.SemaphoreType.DMA((n, 3)), pltpu.SemaphoreType.DMA((n, 3)),
                        pltpu.SemaphoreType.DMA((n,))],
    )(*shards, *lands)
    return list(out)


def _send_other_half(arrs, name):
    n = len(arrs)

    def body(*refs):
        ins, outs = refs[:n], refs[n:2 * n]
        send_sems, recv_sems = refs[2 * n:]
        x, y, c, _ = _place()
        cps = []
        for k in range(n):
            cp = pltpu.make_async_remote_copy(
                src_ref=_half_of(ins[k], 1 - c), dst_ref=outs[k],
                send_sem=send_sems.at[k], recv_sem=recv_sems.at[k], device_id=(x, y, 1 - c), device_id_type=MESH)
            cp.start()
            cps.append(cp)
        for cp in cps:
            cp.wait()

    return pl.pallas_call(
        body, name=name,
        out_shape=[jax.ShapeDtypeStruct(_half_shape(a.shape), a.dtype) for a in arrs],
        in_specs=[ANY_SPEC] * n, out_specs=[ANY_SPEC] * n,
        scratch_shapes=[pltpu.SemaphoreType.DMA((n,)), pltpu.SemaphoreType.DMA((n,))],
    )(*arrs)


def _send_half_start(arrs, name):
    n = len(arrs)

    def body(*refs):
        ins, lnd = refs[:n], refs[n:2 * n]
        send_sems, recv_sems = refs[2 * n:3 * n], refs[3 * n:4 * n]
        token = refs[-1]
        x, y, c, _ = _place()
        for k in range(n):
            pltpu.make_async_remote_copy(
                src_ref=_half_of(ins[k], 1 - c), dst_ref=lnd[k], send_sem=send_sems[k], recv_sem=recv_sems[k],
                device_id=(x, y, 1 - c), device_id_type=MESH).start()
        token[...] = jnp.zeros_like(token)

    lands = [_hbm(lax.empty(_half_shape(a.shape), a.dtype)) for a in arrs]
    out = pl.pallas_call(
        body, name=name,
        out_shape=(*[pltpu.SemaphoreType.DMA(())] * (2 * n), *[pltpu.HBM(a.shape, a.dtype) for a in arrs],
                   *[pltpu.HBM(l.shape, l.dtype) for l in lands], jax.ShapeDtypeStruct((8, 128), F32)),
        in_specs=[HBM_SPEC] * (2 * n),
        out_specs=(*[SEM_SPEC] * (2 * n), *[HBM_SPEC] * (2 * n), VMEM_SPEC),
        input_output_aliases={i: 2 * n + i for i in range(2 * n)},
        compiler_params=pltpu.CompilerParams(has_side_effects=EFFECT),
    )(*[_hbm(a) for a in arrs], *lands)
    return list(out[:2 * n]), list(out[2 * n:3 * n]), list(out[3 * n:4 * n]), out[-1]


def _send_half_wait(sems, arrs, lands, after, name):
    n = len(arrs)

    def body(*refs):
        ins, lnd = refs[:n], refs[n:2 * n]
        s_sems, r_sems = refs[2 * n:3 * n], refs[3 * n:4 * n]
        x, y, c, _ = _place()
        for k in range(n):
            cp = pltpu.make_async_remote_copy(
                src_ref=_half_of(ins[k], 1 - c), dst_ref=lnd[k], send_sem=s_sems[k], recv_sem=r_sems[k],
                device_id=(x, y, 1 - c), device_id_type=MESH)
            cp.wait_send()
            cp.wait_recv()

    out = pl.pallas_call(
        body, name=name,
        out_shape=tuple(pltpu.HBM(a.shape, a.dtype) for a in list(arrs) + list(lands)),
        in_specs=[HBM_SPEC] * (2 * n) + [SEM_SPEC] * (2 * n) + [ANY_SPEC] * len(after),
        out_specs=[HBM_SPEC] * (2 * n),
        input_output_aliases={i: i for i in range(2 * n)},
        compiler_params=pltpu.CompilerParams(has_side_effects=EFFECT),
    )(*arrs, *lands, *sems, *after)
    return list(out[:n]), list(out[n:])


def _blocks_to_owner(arrs, name):
    n = len(arrs)

    def body(*refs):
        ins, outs = refs[:n], refs[n:2 * n]
        send_sems, recv_sems, local_sems = refs[2 * n:]
        x, y, c, chips = _place()
        me_chip = 2 * x + y
        locals_, started = [], []
        for k in range(n):
            cp = pltpu.make_async_copy(ins[k].at[me_chip], outs[k].at[me_chip], local_sems.at[k])
            cp.start()
            locals_.append(cp)

        def copy(k, j, src_block, dst_slot, to):
            return pltpu.make_async_remote_copy(
                src_ref=ins[k].at[src_block], dst_ref=outs[k].at[dst_slot],
                send_sem=send_sems.at[k, j], recv_sem=recv_sems.at[k, j], device_id=to, device_id_type=MESH)

        for k in range(n):
            for j, (px, py) in enumerate(chips):
                cp = copy(k, j, 2 * px + py, me_chip, (px, py, c))
                cp.start()
                started.append(cp)
        for k in range(n):
            for j, (px, py) in enumerate(chips):
                copy(k, j, me_chip, 2 * px + py, (px, py, c)).wait_recv()
        for cp in started:
            cp.wait_send()
        for cp in locals_:
            cp.wait()

    return pl.pallas_call(
        body, name=name,
        out_shape=[jax.ShapeDtypeStruct(a.shape, a.dtype) for a in arrs],
        in_specs=[ANY_SPEC] * n, out_specs=[ANY_SPEC] * n,
        scratch_shapes=[pltpu.SemaphoreType.DMA((n, 3)), pltpu.SemaphoreType.DMA((n, 3)),
                        pltpu.SemaphoreType.DMA((n,))],
    )(*arrs)


def _scatter_blocks(arrs, name):
    n = len(arrs)

    def body(*refs):
        ins, outs = refs[:n], refs[n:2 * n]
        send_sems, recv_sems = refs[2 * n:]
        x, y, c, chips = _place()
        me_chip = 2 * x + y

        def copy(k, j, src_block, dst_slot, to):
            return pltpu.make_async_remote_copy(
                src_ref=ins[k].at[src_block], dst_ref=outs[k].at[dst_slot],
                send_sem=send_sems.at[k, j], recv_sem=recv_sems.at[k, j], device_id=to, device_id_type=MESH)

        started = []
        for k in range(n):
            for j, (px, py) in enumerate(chips):
                cp = copy(k, j, 2 * px + py, me_chip, (px, py, c))
                cp.start()
                started.append(cp)
        for k in range(n):
            for j, (px, py) in enumerate(chips):
                copy(k, j, me_chip, 2 * px + py, (px, py, c)).wait_recv()
        for cp in started:
            cp.wait_send()

    return pl.pallas_call(
        body, name=name,
        out_shape=[jax.ShapeDtypeStruct(a.shape, a.dtype) for a in arrs],
        in_specs=[ANY_SPEC] * n, out_specs=[ANY_SPEC] * n,
        scratch_shapes=[pltpu.SemaphoreType.DMA((n, 3)), pltpu.SemaphoreType.DMA((n, 3))],
    )(*arrs)


def _scatter_start(arrs, name, after=()):
    n = len(arrs)
    n_sem = 3 * n
    first = 2 * n + len(after)

    def body(*refs):
        ins, lnd = refs[:n], refs[n:2 * n]
        send_sems, recv_sems = refs[first:first + n_sem], refs[first + n_sem:first + 2 * n_sem]
        token = refs[-1]
        x, y, c, chips = _place()
        me_chip = 2 * x + y
        for k in range(n):
            for j, (px, py) in enumerate(chips):
                pltpu.make_async_remote_copy(
                    src_ref=ins[k].at[2 * px + py], dst_ref=lnd[k].at[me_chip],
                    send_sem=send_sems[3 * k + j], recv_sem=recv_sems[3 * k + j],
                    device_id=(px, py, c), device_id_type=MESH).start()
        token[...] = jnp.zeros_like(token)

    lands = [_hbm(lax.empty(a.shape, a.dtype)) for a in arrs]
    out = pl.pallas_call(
        body, name=name,
        out_shape=(*[pltpu.SemaphoreType.DMA(())] * (2 * n_sem),
                   *[pltpu.HBM(a.shape, a.dtype) for a in arrs], *[pltpu.HBM(a.shape, a.dtype) for a in arrs],
                   jax.ShapeDtypeStruct((8, 128), F32)),
        in_specs=[HBM_SPEC] * (2 * n) + [ANY_SPEC] * len(after),
        out_specs=(*[SEM_SPEC] * (2 * n_sem), *[HBM_SPEC] * (2 * n), VMEM_SPEC),
        input_output_aliases={i: 2 * n_sem + i for i in range(2 * n)},
        compiler_params=pltpu.CompilerParams(has_side_effects=EFFECT),
    )(*[_hbm(a) for a in arrs], *lands, *after)
    base = 2 * n_sem
    return list(out[:base]), list(out[base:base + n]), list(out[base + n:base + 2 * n]), out[-1]


def _scatter_wait(sems, arrs, lands, after, name):
    n = len(arrs)
    n_sem = 3 * n

    def body(*refs):
        ins, lnd = refs[:n], refs[n:2 * n]
        s_sems, r_sems = refs[2 * n:2 * n + n_sem], refs[2 * n + n_sem:2 * n + 2 * n_sem]
        x, y, c, chips = _place()
        for k in range(n):
            for j, (px, py) in enumerate(chips):
                cp = pltpu.make_async_remote_copy(
                    src_ref=ins[k].at[2 * px + py], dst_ref=lnd[k].at[2 * px + py],
                    send_sem=s_sems[3 * k + j], recv_sem=r_sems[3 * k + j],
                    device_id=(px, py, c), device_id_type=MESH)
                cp.wait_send()
                cp.wait_recv()

    out = pl.pallas_call(
        body, name=name,
        out_shape=tuple(pltpu.HBM(a.shape, a.dtype) for a in list(arrs) + list(lands)),
        in_specs=[HBM_SPEC] * (2 * n) + [SEM_SPEC] * (2 * n_sem) + [ANY_SPEC] * len(after),
        out_specs=[HBM_SPEC] * (2 * n),
        input_output_aliases={i: i for i in range(2 * n)},
        compiler_params=pltpu.CompilerParams(has_side_effects=EFFECT),
    )(*arrs, *lands, *sems, *after)
    return list(out[:n]), list(out[n:])


def _sum_owner(chip_arr, pairs, got, name):
    nb, h, cols = got.shape
    tr = _row_tile(h, 16, 256)

    def body(chip_ref, own_ref, a_ref, b_ref, c_ref, o_ref):
        o_ref[...] = ((own_ref[0].astype(F32) + a_ref[0].astype(F32)) + b_ref[0].astype(F32)) + c_ref[0].astype(F32)

    def slot(off):
        return pl.BlockSpec((1, tr, cols), lambda i, chip_ref: ((chip_ref[0] + off) % N_CHIP, i, 0))

    return pl.pallas_call(
        body, name=name,
        grid_spec=pltpu.PrefetchScalarGridSpec(
            num_scalar_prefetch=1, grid=(h // tr,),
            in_specs=[slot(0), slot(1), slot(2), slot(3)],
            out_specs=pl.BlockSpec((tr, cols), lambda i, chip_ref: (i, 0))),
        out_shape=jax.ShapeDtypeStruct((h, cols), F32),
        compiler_params=_cparams(dimension_semantics=("parallel",)),
    )(chip_arr, pairs, got, got, got)


def _swap_start(arrs, name, after=()):
    n = len(arrs)
    first = 2 * n + len(after)

    def body(*refs):
        ins, lnd = refs[:n], refs[n:2 * n]
        send_sems, recv_sems = refs[first:first + n], refs[first + n:first + 2 * n]
        x, y, c, _ = _place()
        for k in range(n):
            pltpu.make_async_remote_copy(
                src_ref=ins[k], dst_ref=lnd[k], send_sem=send_sems[k], recv_sem=recv_sems[k],
                device_id=(x, y, 1 - c), device_id_type=MESH).start()

    lands = [_hbm(lax.empty(a.shape, a.dtype)) for a in arrs]
    out = pl.pallas_call(
        body, name=name,
        out_shape=(*[pltpu.SemaphoreType.DMA(())] * (2 * n), *[pltpu.HBM(a.shape, a.dtype) for a in arrs],
                   *[pltpu.HBM(a.shape, a.dtype) for a in arrs]),
        in_specs=[HBM_SPEC] * (2 * n) + [ANY_SPEC] * len(after),
        out_specs=(*[SEM_SPEC] * (2 * n), *[HBM_SPEC] * (2 * n)),
        input_output_aliases={i: 2 * n + i for i in range(2 * n)},
        compiler_params=pltpu.CompilerParams(has_side_effects=EFFECT),
    )(*[_hbm(a) for a in arrs], *lands, *after)
    return list(out[:2 * n]), list(out[2 * n:3 * n]), list(out[3 * n:4 * n])


def _swap_wait(sems, arrs, lands, after, name):
    n = len(arrs)

    def body(*refs):
        ins, lnd = refs[:n], refs[n:2 * n]
        s_sems, r_sems = refs[2 * n:3 * n], refs[3 * n:4 * n]
        x, y, c, _ = _place()
        for k in range(n):
            cp = pltpu.make_async_remote_copy(
                src_ref=ins[k], dst_ref=lnd[k], send_sem=s_sems[k], recv_sem=r_sems[k],
                device_id=(x, y, 1 - c), device_id_type=MESH)
            cp.wait_send()
            cp.wait_recv()

    out = pl.pallas_call(
        body, name=name,
        out_shape=tuple(pltpu.HBM(a.shape, a.dtype) for a in list(arrs) + list(lands)),
        in_specs=[HBM_SPEC] * (2 * n) + [SEM_SPEC] * (2 * n) + [ANY_SPEC] * len(after),
        out_specs=[HBM_SPEC] * (2 * n),
        input_output_aliases={i: i for i in range(2 * n)},
        compiler_params=pltpu.CompilerParams(has_side_effects=EFFECT),
    )(*arrs, *lands, *sems, *after)
    return list(out[:n]), list(out[n:])


def _swap_with_sibling(arrs, name):
    n = len(arrs)

    def body(*refs):
        ins, outs = refs[:n], refs[n:2 * n]
        send_sems, recv_sems = refs[2 * n:]
        x, y, c, _ = _place()
        cps = []
        for k in range(n):
            cp = pltpu.make_async_remote_copy(
                src_ref=ins[k], dst_ref=outs[k], send_sem=send_sems.at[k], recv_sem=recv_sems.at[k],
                device_id=(x, y, 1 - c), device_id_type=MESH)
            cp.start()
            cps.append(cp)
        for cp in cps:
            cp.wait()

    return pl.pallas_call(
        body, name=name,
        out_shape=[jax.ShapeDtypeStruct(a.shape, a.dtype) for a in arrs],
        in_specs=[ANY_SPEC] * n, out_specs=[ANY_SPEC] * n,
        scratch_shapes=[pltpu.SemaphoreType.DMA((n,)), pltpu.SemaphoreType.DMA((n,))],
    )(*arrs)


def _row_tile(h, mult=8, cap=128):
    for t in range(cap - cap % mult, mult - 1, -mult):
        if h % t == 0:
            return t
    if mult > 8:
        return _row_tile(h, 8, cap)
    raise ValueError(h)


def _cast_bf16(a, name):
    rows, cols = a.shape
    tr = _row_tile(rows, 16, 256)

    def body(a_ref, o_ref):
        o_ref[...] = a_ref[...].astype(BF16)

    return pl.pallas_call(
        body, name=name, grid=(rows // tr,),
        out_shape=jax.ShapeDtypeStruct(a.shape, BF16),
        in_specs=[pl.BlockSpec((tr, cols), lambda i: (i, 0))],
        out_specs=pl.BlockSpec((tr, cols), lambda i: (i, 0)),
        compiler_params=_cparams(dimension_semantics=("parallel",)),
    )(a)


def _pair_sum(c_arr, full, recv, name):
    nb, rows, cols = full.shape

    def body(c_ref, f_ref, r_ref, o_ref):
        o_ref[...] = (f_ref[...] + r_ref[...]).astype(BF16)

    if _halved_by_rows(full.shape):
        h = rows // 2
        tr = _row_tile(h, 16, 256)
        steps = h // tr
        own = pl.BlockSpec((1, tr, cols), lambda b, i, c_ref: (b, c_ref[0] * steps + i, 0))
        half = pl.BlockSpec((1, tr, cols), lambda b, i, c_ref: (b, i, 0))
    else:
        steps = 1
        own = pl.BlockSpec((1, rows, cols // 2), lambda b, i, c_ref: (b, 0, c_ref[0]))
        half = pl.BlockSpec((1, rows, cols // 2), lambda b, i, c_ref: (b, 0, 0))
    return pl.pallas_call(
        body, name=name,
        grid_spec=pltpu.PrefetchScalarGridSpec(
            num_scalar_prefetch=1, grid=(nb, steps), in_specs=[own, half], out_specs=half),
        out_shape=jax.ShapeDtypeStruct(_half_shape(full.shape), BF16),
        compiler_params=_cparams(dimension_semantics=("parallel", "parallel")),
    )(c_arr, full, recv)


def _sum_chips(got, name):
    nb, h, cols = got.shape
    tr = _row_tile(h, 16, 256)

    def body(g_ref, o_ref):
        g = g_ref[...].astype(F32)
        o_ref[...] = ((g[0] + g[1]) + g[2]) + g[3]

    return pl.pallas_call(
        body, name=name, grid=(h // tr,),
        out_shape=jax.ShapeDtypeStruct((h, cols), F32),
        in_specs=[pl.BlockSpec((nb, tr, cols), lambda i: (0, i, 0))],
        out_specs=pl.BlockSpec((tr, cols), lambda i: (i, 0)),
        compiler_params=_cparams(dimension_semantics=("parallel",)),
    )(got)


def _adam_math(g, w, m, v):
    m1 = ADAM_B1 * m + (1.0 - ADAM_B1) * g
    v1 = ADAM_B2 * v + (1.0 - ADAM_B2) * (g * g)
    m_hat = m1 / (1.0 - ADAM_B1 ** ADAM_STEP)
    v_hat = v1 / (1.0 - ADAM_B2 ** ADAM_STEP)
    delta = -ADAM_LR * (m_hat / (jnp.sqrt(v_hat) + ADAM_EPS) + ADAM_WD * w)
    return delta, m1, v1


def _adamw_halves(c_arr, own, other, w, m, v, name):
    rows, cols = w.shape
    by_rows = own.shape[1] == cols

    def body(c_ref, own_ref, oth_ref, w_ref, m_ref, v_ref, g_out, d_out, m_out, v_out):
        if by_rows:
            g = jnp.where(pl.program_id(0) == c_ref[0], own_ref[...], oth_ref[...])
        else:
            own_, oth_ = own_ref[...], oth_ref[...]
            g = jnp.where(c_ref[0] == 0, jnp.concatenate([own_, oth_], axis=1), jnp.concatenate([oth_, own_], axis=1))
        d, m1, v1 = _adam_math(g, w_ref[...], m_ref[...], v_ref[...])
        g_out[...] = g
        d_out[...] = d
        m_out[...] = m1
        v_out[...] = v1

    if by_rows:
        h = rows // 2
        tr = _row_tile(h)
        steps = h // tr
        grid = (2, steps)
        half_spec = pl.BlockSpec((tr, cols), lambda p, i, c_ref: (i, 0))
        full_spec = pl.BlockSpec((tr, cols), lambda p, i, c_ref: (p * steps + i, 0))
    else:
        tr = _row_tile(rows)
        grid = (1, rows // tr)
        half_spec = pl.BlockSpec((tr, cols // 2), lambda p, i, c_ref: (i, 0))
        full_spec = pl.BlockSpec((tr, cols), lambda p, i, c_ref: (i, 0))
    return pl.pallas_call(
        body, name=name,
        grid_spec=pltpu.PrefetchScalarGridSpec(
            num_scalar_prefetch=1, grid=grid,
            in_specs=[half_spec, half_spec, full_spec, full_spec, full_spec],
            out_specs=[full_spec] * 4),
        out_shape=[jax.ShapeDtypeStruct(w.shape, F32)] * 4,
        compiler_params=_cparams(dimension_semantics=("parallel", "parallel")),
    )(c_arr, own, other, w, m, v)


def _adamw_whole(items, name):
    n = len(items)

    def body(*refs):
        ins, outs = refs[:4 * n], refs[4 * n:]
        for k in range(n):
            g, w, m, v = (r[...] for r in ins[4 * k:4 * k + 4])
            d, m1, v1 = _adam_math(g, w, m, v)
            outs[3 * k][...] = d
            outs[3 * k + 1][...] = m1
            outs[3 * k + 2][...] = v1

    flat = [a for it in items for a in it]
    shapes = [jax.ShapeDtypeStruct(it[1].shape, F32) for it in items for _ in range(3)]
    out = pl.pallas_call(
        body, name=name, out_shape=shapes,
        in_specs=[VMEM_SPEC] * (4 * n), out_specs=[VMEM_SPEC] * (3 * n),
        compiler_params=_cparams(),
    )(*flat)
    return [tuple(out[3 * k:3 * k + 3]) for k in range(n)]


def _adamw_tiled(g, w, m, v, name):
    rows, cols = w.shape
    tr = _row_tile(rows)

    def body(g_ref, w_ref, m_ref, v_ref, d_out, m_out, v_out):
        d, m1, v1 = _adam_math(g_ref[...], w_ref[...], m_ref[...], v_ref[...])
        d_out[...] = d
        m_out[...] = m1
        v_out[...] = v1

    spec = pl.BlockSpec((tr, cols), lambda i: (i, 0))
    return pl.pallas_call(
        body, name=name, grid=(rows // tr,),
        out_shape=[jax.ShapeDtypeStruct(w.shape, F32)] * 3,
        in_specs=[spec] * 4, out_specs=[spec] * 3,
        compiler_params=_cparams(dimension_semantics=("parallel",)),
    )(g, w, m, v)


def _mod_forward(cond, w_mod, b_mod_cols, name):
    def body(c_ref, w_ref, b_ref, o_ref):
        o_ref[...] = _dot(_silu(c_ref[...]), w_ref[...]) + b_ref[...]

    return pl.pallas_call(
        body, name=name, out_shape=jax.ShapeDtypeStruct((cond.shape[0], w_mod.shape[1]), F32),
        in_specs=[VMEM_SPEC] * 3, out_specs=VMEM_SPEC, compiler_params=_cparams(),
    )(cond, w_mod, b_mod_cols)


def _mod_backward(cond, w_mod, dmod_cols, name):
    def body(c_ref, w_ref, d_ref, gw_ref, gc_ref):
        s = _silu(c_ref[...])
        d = d_ref[...]
        gw_ref[...] = _dot_tn(s, d)
        gc_ref[...] = _dot_nt(d[8:16, :], w_ref[...])

    return pl.pallas_call(
        body, name=name,
        out_shape=[jax.ShapeDtypeStruct(w_mod.shape, F32), jax.ShapeDtypeStruct((8, w_mod.shape[0]), F32)],
        in_specs=[VMEM_SPEC] * 3, out_specs=[VMEM_SPEC] * 2, compiler_params=_cparams(),
    )(cond, w_mod, dmod_cols)


def _col_chunks(width, step=512):
    return [(s, min(step, width - s)) for s in range(0, width, step)]


def _w_in_row(p_off):
    if p_off < 9 * HW:
        return p_off
    return 9 * HW if p_off == OFF_LR else p_off + 2 * RANK


def _in_projection(ctx0, x0, modc, modx, pre1, w_t, n_ctx_tiles, name):
    d = x0.shape[1]
    rows = ctx0.shape[0] + x0.shape[0]
    width = P_WIDTH

    def body(ctx_ref, x_ref, modc_ref, modx_ref, pre_ref, w_ref, h_ref, p_ref):
        is_ctx = pl.program_id(0) < n_ctx_tiles
        n, _ = _rms(jnp.where(is_ctx, ctx_ref[...], x_ref[...]))
        shift = jnp.where(is_ctx, modc_ref[0:1, :], modx_ref[0:1, :])
        scale = jnp.where(is_ctx, modc_ref[1:2, :], modx_ref[1:2, :])
        h = (n * pre_ref[...] * (1.0 + scale) + shift).astype(BF16)
        h_ref[...] = h
        for s, w in _col_chunks(width):
            p_ref[:, s:s + w] = _dot_nt(h, w_ref[_w_in_row(s):_w_in_row(s) + w, :])

    row = lambda i: (i, 0)
    fixed = lambda i: (0, 0)
    return pl.pallas_call(
        body, name=name, grid=(rows // TM,),
        out_shape=[jax.ShapeDtypeStruct((rows, d), BF16), jax.ShapeDtypeStruct((rows, width), F32)],
        in_specs=[pl.BlockSpec((TM, d), lambda i: (jnp.minimum(i, n_ctx_tiles - 1), 0)),
                  pl.BlockSpec((TM, d), lambda i: (jnp.maximum(i - n_ctx_tiles, 0), 0)),
                  pl.BlockSpec((8, d), fixed), pl.BlockSpec((8, d), fixed), pl.BlockSpec((1, d), fixed), VMEM_SPEC],
        out_specs=[pl.BlockSpec((TM, d), row), pl.BlockSpec((TM, width), row)],
        compiler_params=_cparams(dimension_semantics=("parallel",)),
    )(ctx0, x0, modc, modx, pre1, w_t)


C_HQ, C_HI, C_HF_FW, C_HF_BW, C_HGATE, C_GQ, C_GK, C_GV, C_GGATE = range(9)
OFF_GATE_HG = 9 * HW
OFF_LR = 13 * HW
P_WIDTH = OFF_LR + 128


def _head_norm_fwd(o, w):
    outs, ns, rs = [], [], []
    for h in range(NH):
        n, r = _rms(o[:, h * HD:(h + 1) * HD])
        ns.append(n)
        rs.append(r)
        outs.append(n * w)
    return jnp.concatenate(outs, axis=1), ns, rs


def _mixer_tail(z, o_hg, o_gla, p_hgate, p_ggate, p_gate_hg, p_gate_gla, hg_on, gla_on, wbh, wbg, wout):
    on_hg, n_hg, r_hg = _head_norm_fwd(o_hg, hg_on)
    on_gla, n_gla, r_gla = _head_norm_fwd(o_gla, gla_on)
    og_hg = (on_hg * _silu(p_hgate)).astype(BF16)
    og_gla = (on_gla * _silu(p_ggate)).astype(BF16)
    b_hg = jnp.dot(og_hg, wbh, preferred_element_type=F32)
    b_gla = jnp.dot(og_gla, wbg, preferred_element_type=F32)
    s_hg = _sigmoid(p_gate_hg)
    s_gla = _sigmoid(p_gate_gla)
    merged = (s_hg * b_hg + s_gla * b_gla).astype(BF16)
    y1 = jnp.dot(merged, wout, preferred_element_type=F32)
    return dict(on_hg=on_hg, n_hg=n_hg, r_hg=r_hg, on_gla=on_gla, n_gla=n_gla, r_gla=r_gla, og_hg=og_hg,
                og_gla=og_gla, b_hg=b_hg, b_gla=b_gla, s_hg=s_hg, s_gla=s_gla, merged=merged, y1=y1)


def _mixer_ffn(x_lat, p, o_list, modx, norms, onorms, w_br_hg, w_br_gla, w_out, w_gate, w_up, w_down, target,
               n_ctx_tiles, name):
    rows, d = x_lat.shape
    dff = w_gate.shape[0]
    inv_d = 1.0 / d

    def body(x_ref, ofw_hg, obw_hg, ofw_gla, obw_gla, p_hgate, p_ggate, p_ghg_a, p_ghg_b, p_ggla_a, p_ggla_b,
             modx_ref, norm_ref, on_ref, wbh_ref, wbg_ref, wout_ref, wg_ref, wu_ref, wd_ref, t_ref,
             loss_ref, dz2_ref, y1_ref, mrg_ref, oghg_ref, oggla_ref, h2_ref, a_ref, du_ref, dv_ref, dy2_ref,
             stat_ref):
        i = pl.program_id(0)
        post1, pre2, post2 = norm_ref[1:2, :], norm_ref[2:3, :], norm_ref[3:4, :]
        gate1, shift2, scale2, gate2 = modx_ref[2:3, :], modx_ref[3:4, :], modx_ref[4:5, :], modx_ref[5:6, :]
        p_gate_hg = jnp.concatenate([p_ghg_a[...], p_ghg_b[...]], axis=1)
        p_gate_gla = jnp.concatenate([p_ggla_a[...], p_ggla_b[...]], axis=1)
        t = _mixer_tail(x_ref[...], ofw_hg[...] + obw_hg[...], ofw_gla[...] + obw_gla[...], p_hgate[...],
                        p_ggate[...], p_gate_hg, p_gate_gla, on_ref[0:1, 0:HD], on_ref[1:2, 0:HD],
                        wbh_ref[...], wbg_ref[...], wout_ref[...])
        y1_ref[...] = t["y1"]
        mrg_ref[...] = t["merged"]
        oghg_ref[...] = t["og_hg"]
        oggla_ref[...] = t["og_gla"]
        n1, _ = _rms(t["y1"])
        z2 = x_ref[...] + n1 * post1 * gate1
        n2, r2 = _rms(z2)
        nw2 = n2 * pre2
        h2 = (nw2 * (1.0 + scale2) + shift2).astype(BF16)
        h2_ref[...] = h2
        u = _dot_nt(h2, wg_ref[...])
        v = _dot_nt(h2, wu_ref[...])
        su = _silu(u)
        a = (su * v).astype(BF16)
        a_ref[...] = a
        y2 = jnp.dot(a, wd_ref[...], preferred_element_type=F32)
        n3, r3 = _rms(y2)
        z3 = z2 + n3 * post2 * gate2
        err = z3 - t_ref[...]
        part = 0.5 * inv_d * jnp.sum(err * err)
        dz3 = err * inv_d
        dgate2 = _colsum(dz3 * n3 * post2)
        tt = dz3 * gate2
        dpost2 = _colsum(tt * n3)
        dy2 = _rms_bwd(tt * post2, n3, r3).astype(BF16)
        dy2_ref[...] = dy2
        da = _dot_nt(dy2, wd_ref[...])
        du = (da * v * _dsilu(u)).astype(BF16)
        dv = (da * su).astype(BF16)
        du_ref[...] = du
        dv_ref[...] = dv
        dh2 = (jnp.dot(du, wg_ref[...], preferred_element_type=F32)
               + jnp.dot(dv, wu_ref[...], preferred_element_type=F32))
        dshift2 = _colsum(dh2)
        dscale2 = _colsum(dh2 * nw2)
        dnw2 = dh2 * (1.0 + scale2)
        dpre2 = _colsum(dnw2 * n2)
        dz2_ref[...] = dz3 + _rms_bwd(dnw2 * pre2, n2, r2)

        @pl.when(i == 0)
        def _():
            stat_ref[...] = jnp.zeros_like(stat_ref)
            loss_ref[...] = jnp.zeros_like(loss_ref)

        for r, val in enumerate((dshift2, dscale2, dgate2, dpre2, dpost2)):
            stat_ref[r:r + 1, :] += val
        loss_ref[...] += part
        stat_ref[5:6, 0:128] += part

    tm = TM_FFN
    ctx_tiles = n_ctx_tiles * (TM // tm)
    lat = lambda i: (i, 0)
    full = lambda i: (i + ctx_tiles, 0)
    fixed = lambda i: (0, 0)

    def pcol(blk):
        return pl.BlockSpec((tm, HW), lambda i: (i + ctx_tiles, blk))

    in_specs = ([pl.BlockSpec((tm, d), lat)] + [pl.BlockSpec((tm, HW), full)] * 4
                + [pcol(C_HGATE), pcol(C_GGATE), pcol(9), pcol(10), pcol(11), pcol(12)]
                + [pl.BlockSpec((8, d), fixed), pl.BlockSpec((8, d), fixed), pl.BlockSpec((8, d), fixed)]
                + [VMEM_SPEC] * 6 + [pl.BlockSpec((tm, d), lat)])
    bf = lambda w: jax.ShapeDtypeStruct((rows, w), BF16)
    out_shape = [jax.ShapeDtypeStruct((8, 128), F32), jax.ShapeDtypeStruct((rows, d), F32),
                 jax.ShapeDtypeStruct((rows, d), F32), bf(d), bf(HW), bf(HW), bf(d), bf(dff), bf(dff), bf(dff), bf(d),
                 jax.ShapeDtypeStruct((8, d), F32)]
    out_specs = [pl.BlockSpec((8, 128), fixed), pl.BlockSpec((tm, d), lat), pl.BlockSpec((tm, d), lat),
                 pl.BlockSpec((tm, d), lat), pl.BlockSpec((tm, HW), lat), pl.BlockSpec((tm, HW), lat),
                 pl.BlockSpec((tm, d), lat), pl.BlockSpec((tm, dff), lat), pl.BlockSpec((tm, dff), lat),
                 pl.BlockSpec((tm, dff), lat), pl.BlockSpec((tm, d), lat), pl.BlockSpec((8, d), fixed)]
    return pl.pallas_call(
        body, name=name, grid=(rows // tm,), out_shape=out_shape, in_specs=in_specs, out_specs=out_specs,
        compiler_params=_cparams(dimension_semantics=("arbitrary",)),
    )(x_lat, *o_list, p, p, p, p, p, p, modx, norms, onorms, w_br_hg, w_br_gla, w_out, w_gate, w_up, w_down, target)


def _mixer_tail_fwd(x_lat, p, o_list, modx, norms, onorms, w_br_hg, w_br_gla, w_out, n_ctx_tiles, name):
    rows, d = x_lat.shape

    def body(x_ref, ofw_hg, obw_hg, ofw_gla, obw_gla, p_hgate, p_ggate, p_ghg_a, p_ghg_b, p_ggla_a, p_ggla_b,
             modx_ref, norm_ref, on_ref, wbh_ref, wbg_ref, wout_ref, z2_ref, y1_ref, mrg_ref, oghg_ref, oggla_ref):
        p_gate_hg = jnp.concatenate([p_ghg_a[...], p_ghg_b[...]], axis=1)
        p_gate_gla = jnp.concatenate([p_ggla_a[...], p_ggla_b[...]], axis=1)
        t = _mixer_tail(x_ref[...], ofw_hg[...] + obw_hg[...], ofw_gla[...] + obw_gla[...], p_hgate[...],
                        p_ggate[...], p_gate_hg, p_gate_gla, on_ref[0:1, 0:HD], on_ref[1:2, 0:HD],
                        wbh_ref[...], wbg_ref[...], wout_ref[...])
        y1_ref[...] = t["y1"]
        mrg_ref[...] = t["merged"]
        oghg_ref[...] = t["og_hg"]
        oggla_ref[...] = t["og_gla"]
        n1, _ = _rms(t["y1"])
        z2_ref[...] = x_ref[...] + n1 * norm_ref[1:2, :] * modx_ref[2:3, :]

    lat = lambda i: (i, 0)
    full = lambda i: (i + n_ctx_tiles, 0)
    fixed = lambda i: (0, 0)

    def pcol(blk):
        return pl.BlockSpec((TM, HW), lambda i: (i + n_ctx_tiles, blk))

    in_specs = ([pl.BlockSpec((TM, d), lat)] + [pl.BlockSpec((TM, HW), full)] * 4
                + [pcol(C_HGATE), pcol(C_GGATE), pcol(9), pcol(10), pcol(11), pcol(12)]
                + [pl.BlockSpec((8, d), fixed)] * 3 + [VMEM_SPEC] * 3)
    bf = lambda w: jax.ShapeDtypeStruct((rows, w), BF16)
    f32 = jax.ShapeDtypeStruct((rows, d), F32)
    return pl.pallas_call(
        body, name=name, grid=(rows // TM,), out_shape=[f32, f32, bf(d), bf(HW), bf(HW)], in_specs=in_specs,
        out_specs=[pl.BlockSpec((TM, d), lat)] * 3 + [pl.BlockSpec((TM, HW), lat)] * 2,
        compiler_params=_cparams(dimension_semantics=("parallel",)),
    )(x_lat, *o_list, p, p, p, p, p, p, modx, norms, onorms, w_br_hg, w_br_gla, w_out)


def _ffn_fwd_bwd(z2, modx, norms, w_gate, w_up, w_down, target, name):
    rows, d = z2.shape
    dff = w_gate.shape[0]
    inv_d = 1.0 / d

    def body(z2_ref, modx_ref, norm_ref, wg_ref, wu_ref, wd_ref, t_ref,
             loss_ref, dz2_ref, h2_ref, a_ref, du_ref, dv_ref, dy2_ref, stat_ref):
        i = pl.program_id(0)
        pre2, post2 = norm_ref[2:3, :], norm_ref[3:4, :]
        shift2, scale2, gate2 = modx_ref[3:4, :], modx_ref[4:5, :], modx_ref[5:6, :]
        z2 = z2_ref[...]
        n2, r2 = _rms(z2)
        nw2 = n2 * pre2
        h2 = (nw2 * (1.0 + scale2) + shift2).astype(BF16)
        h2_ref[...] = h2
        u = _dot_nt(h2, wg_ref[...])
        v = _dot_nt(h2, wu_ref[...])
        su = _silu(u)
        a = (su * v).astype(BF16)
        a_ref[...] = a
        y2 = jnp.dot(a, wd_ref[...], preferred_element_type=F32)
        n3, r3 = _rms(y2)
        err = z2 + n3 * post2 * gate2 - t_ref[...]
        part = 0.5 * inv_d * jnp.sum(err * err)
        dz3 = err * inv_d
        dgate2 = _colsum(dz3 * n3 * post2)
        tt = dz3 * gate2
        dpost2 = _colsum(tt * n3)
        dy2 = _rms_bwd(tt * post2, n3, r3).astype(BF16)
        dy2_ref[...] = dy2
        da = _dot_nt(dy2, wd_ref[...])
        du = (da * v * _dsilu(u)).astype(BF16)
        dv = (da * su).astype(BF16)
        du_ref[...] = du
        dv_ref[...] = dv
        dh2 = (jnp.dot(du, wg_ref[...], preferred_element_type=F32)
               + jnp.dot(dv, wu_ref[...], preferred_element_type=F32))
        dshift2 = _colsum(dh2)
        dscale2 = _colsum(dh2 * nw2)
        dnw2 = dh2 * (1.0 + scale2)
        dpre2 = _colsum(dnw2 * n2)
        dz2_ref[...] = dz3 + _rms_bwd(dnw2 * pre2, n2, r2)

        @pl.when(i == 0)
        def _():
            stat_ref[...] = jnp.zeros_like(stat_ref)
            loss_ref[...] = jnp.zeros_like(loss_ref)

        for r, val in enumerate((dshift2, dscale2, dgate2, dpre2, dpost2)):
            stat_ref[r:r + 1, :] += val
        loss_ref[...] += part
        stat_ref[5:6, 0:128] += part

    lat = lambda i: (i, 0)
    fixed = lambda i: (0, 0)
    bf = lambda w: jax.ShapeDtypeStruct((rows, w), BF16)
    return pl.pallas_call(
        body, name=name, grid=(rows // TM,),
        out_shape=[jax.ShapeDtypeStruct((8, 128), F32), jax.ShapeDtypeStruct((rows, d), F32), bf(d), bf(dff), bf(dff),
                   bf(dff), bf(d), jax.ShapeDtypeStruct((8, d), F32)],
        in_specs=[pl.BlockSpec((TM, d), lat), pl.BlockSpec((8, d), fixed), pl.BlockSpec((8, d), fixed)]
        + [VMEM_SPEC] * 3 + [pl.BlockSpec((TM, d), lat)],
        out_specs=[pl.BlockSpec((8, 128), fixed), pl.BlockSpec((TM, d), lat), pl.BlockSpec((TM, d), lat),
                   pl.BlockSpec((TM, dff), lat), pl.BlockSpec((TM, dff), lat), pl.BlockSpec((TM, dff), lat),
                   pl.BlockSpec((TM, d), lat), pl.BlockSpec((8, d), fixed)],
        compiler_params=_cparams(dimension_semantics=("arbitrary",)),
    )(z2, modx, norms, w_gate, w_up, w_down, target)


def _mixer_tail_bwd(x_lat, p, o_list, dz2, y1, modx, norms, onorms, w_br_hg, w_br_gla, w_out, n_ctx_tiles, n_tiles,
                    name):
    rows, d = x_lat.shape
    total = n_tiles * TM

    def body(x_ref, ofw_hg, obw_hg, ofw_gla, obw_gla, p_hgate, p_ggate, p_ghg_a, p_ghg_b, p_ggla_a, p_ggla_b,
             dz2_ref, y1_ref, modx_ref, norm_ref, on_ref, wbh_ref, wbg_ref, wout_ref,
             dohg_ref, dogla_ref, dhgate_ref, dggate_ref, dghg_ref, dggla_ref, dy1_ref, dbhg_ref, dbgla_ref,
             stat_ref):
        i = pl.program_id(0)

        @pl.when(i == 0)
        def _():
            stat_ref[...] = jnp.zeros_like(stat_ref)

        @pl.when(i < n_ctx_tiles)
        def _():
            for ref in (dohg_ref, dogla_ref, dhgate_ref, dggate_ref, dghg_ref, dggla_ref):
                ref[...] = jnp.zeros_like(ref)

        @pl.when(i >= n_ctx_tiles)
        def _():
            post1, gate1 = norm_ref[1:2, :], modx_ref[2:3, :]
            hg_on, gla_on = on_ref[0:1, 0:HD], on_ref[1:2, 0:HD]
            p_gate_hg = jnp.concatenate([p_ghg_a[...], p_ghg_b[...]], axis=1)
            p_gate_gla = jnp.concatenate([p_ggla_a[...], p_ggla_b[...]], axis=1)
            ph, pg = p_hgate[...], p_ggate[...]
            t = _mixer_tail(x_ref[...], ofw_hg[...] + obw_hg[...], ofw_gla[...] + obw_gla[...], ph, pg,
                            p_gate_hg, p_gate_gla, hg_on, gla_on, wbh_ref[...], wbg_ref[...], wout_ref[...])
            dz2 = dz2_ref[...]
            n1, r1 = _rms(y1_ref[...])
            dgate1 = _colsum(dz2 * n1 * post1)
            tt = dz2 * gate1
            dpost1 = _colsum(tt * n1)
            dy1 = _rms_bwd(tt * post1, n1, r1).astype(BF16)
            dy1_ref[...] = dy1
            dmerged = _dot_nt(dy1, wout_ref[...])
            dghg_ref[...] = dmerged * t["b_hg"] * t["s_hg"] * (1.0 - t["s_hg"])
            dggla_ref[...] = dmerged * t["b_gla"] * t["s_gla"] * (1.0 - t["s_gla"])
            db_hg = (dmerged * t["s_hg"]).astype(BF16)
            db_gla = (dmerged * t["s_gla"]).astype(BF16)
            dbhg_ref[...] = db_hg
            dbgla_ref[...] = db_gla
            don_acc = []
            for (db, wb, pgate, on, ns, rs, gain, gate_ref, do_ref) in (
                    (db_hg, wbh_ref, ph, t["on_hg"], t["n_hg"], t["r_hg"], hg_on, dhgate_ref, dohg_ref),
                    (db_gla, wbg_ref, pg, t["on_gla"], t["n_gla"], t["r_gla"], gla_on, dggate_ref, dogla_ref)):
                dog = _dot_nt(db, wb[...])
                gate_ref[...] = dog * on * _dsilu(pgate)
                don = dog * _silu(pgate)
                acc = jnp.zeros((1, HD), F32)
                for h in range(NH):
                    sl = slice(h * HD, (h + 1) * HD)
                    acc = acc + _colsum(don[:, sl] * ns[h])
                    do_ref[:, sl] = _rms_bwd(don[:, sl] * gain, ns[h], rs[h])
                don_acc.append(acc)
            stat_ref[0:1, :] += dgate1
            stat_ref[1:2, :] += dpost1
            stat_ref[2:3, 0:HD] += don_acc[0]
            stat_ref[2:3, HD:2 * HD] += don_acc[1]

    lat = lambda i: (jnp.maximum(i - n_ctx_tiles, 0), 0)
    full = lambda i: (i, 0)
    fixed = lambda i: (0, 0)

    def pcol(blk):
        return pl.BlockSpec((TM, HW), lambda i: (i, blk))

    in_specs = ([pl.BlockSpec((TM, d), lat)] + [pl.BlockSpec((TM, HW), full)] * 4
                + [pcol(C_HGATE), pcol(C_GGATE), pcol(9), pcol(10), pcol(11), pcol(12)]
                + [pl.BlockSpec((TM, d), lat), pl.BlockSpec((TM, d), lat)]
                + [pl.BlockSpec((8, d), fixed)] * 3 + [VMEM_SPEC] * 3)
    f = lambda w: jax.ShapeDtypeStruct((total, w), F32)
    out_shape = [f(HW), f(HW), f(HW), f(HW), f(d), f(d), jax.ShapeDtypeStruct((rows, d), BF16),
                 jax.ShapeDtypeStruct((rows, d), BF16), jax.ShapeDtypeStruct((rows, d), BF16),
                 jax.ShapeDtypeStruct((8, d), F32)]
    out_specs = ([pl.BlockSpec((TM, HW), full)] * 4 + [pl.BlockSpec((TM, d), full)] * 2
                 + [pl.BlockSpec((TM, d), lat)] * 3 + [pl.BlockSpec((8, d), fixed)])
    return pl.pallas_call(
        body, name=name, grid=(n_tiles,), out_shape=out_shape, in_specs=in_specs, out_specs=out_specs,
        compiler_params=_cparams(dimension_semantics=("arbitrary",)),
    )(x_lat, *o_list, p, p, p, p, p, p, dz2, y1, modx, norms, onorms, w_br_hg, w_br_gla, w_out)


def _in_projection_bwd(ctx0, x0, dz2, modc, modx, pre1, w_t, pieces, n_ctx_tiles, name):
    d = x0.shape[1]
    rows = ctx0.shape[0] + x0.shape[0]
    lat_rows = dz2.shape[0]
    width = P_WIDTH
    n_pieces = len(pieces)

    def body(*refs):
        ctx_ref, x_ref, dz2_ref, modc_ref, modx_ref, pre_ref, w_ref = refs[:7]
        (dhq_f, dhq_b, dhi_f, dhi_b, dhf_f, dhf_b, dhgate, dgq_f, dgq_b, dgk_f, dgk_b, dgv_f, dgv_b, dggate,
         dghg, dggla, dlr_f, dlr_b) = refs[7:7 + n_pieces]
        dp_ref, gx_ref, stat_ref = refs[7 + n_pieces:]
        i = pl.program_id(0)
        is_ctx = i < n_ctx_tiles
        z = jnp.where(is_ctx, ctx_ref[...], x_ref[...])
        sections = [
            (0, dhq_f[...] + dhq_b[...]), (HW, dhi_f[...] + dhi_b[...]), (2 * HW, dhf_f[...]), (3 * HW, dhf_b[...]),
            (4 * HW, dhgate[...]), (5 * HW, dgq_f[...] + dgq_b[...]), (6 * HW, dgk_f[...] + dgk_b[...]),
            (7 * HW, dgv_f[...] + dgv_b[...]), (8 * HW, dggate[...]),
            (9 * HW, dghg[:, 0:HW]), (10 * HW, dghg[:, HW:2 * HW]),
            (11 * HW, dggla[:, 0:HW]), (12 * HW, dggla[:, HW:2 * HW]), (OFF_LR, dlr_f[...] + dlr_b[...])]
        dh = jnp.zeros((TM, d), F32)
        for off, val in sections:
            w = val.shape[1]
            vb = val.astype(BF16)
            dp_ref[:, off:off + w] = vb
            dh = dh + jnp.dot(vb, w_ref[_w_in_row(off):_w_in_row(off) + w, :], preferred_element_type=F32)
        n, r = _rms(z)
        pre = pre_ref[...]
        scale = jnp.where(is_ctx, modc_ref[1:2, :], modx_ref[1:2, :])
        nw = n * pre
        dshift = _colsum(dh)
        dscale = _colsum(dh * nw)
        dnw = dh * (1.0 + scale)
        dpre = _colsum(dnw * n)
        gx_ref[...] = dz2_ref[...] + _rms_bwd(dnw * pre, n, r)
        zero = jnp.zeros((1, d), F32)

        @pl.when(i == 0)
        def _():
            stat_ref[...] = jnp.zeros_like(stat_ref)

        stat_ref[0:1, :] += jnp.where(is_ctx, zero, dshift)
        stat_ref[1:2, :] += jnp.where(is_ctx, zero, dscale)
        stat_ref[2:3, :] += jnp.where(is_ctx, dshift, zero)
        stat_ref[3:4, :] += jnp.where(is_ctx, dscale, zero)
        stat_ref[4:5, :] += dpre

    full = lambda i: (i, 0)
    lat = lambda i: (jnp.maximum(i - n_ctx_tiles, 0), 0)
    fixed = lambda i: (0, 0)
    piece_specs = [pl.BlockSpec((TM, a.shape[1]), full) for a in pieces]
    in_specs = [pl.BlockSpec((TM, d), lambda i: (jnp.minimum(i, n_ctx_tiles - 1), 0)), pl.BlockSpec((TM, d), lat),
                pl.BlockSpec((TM, d), lat), pl.BlockSpec((8, d), fixed),
                pl.BlockSpec((8, d), fixed), pl.BlockSpec((1, d), fixed), VMEM_SPEC] + piece_specs
    return pl.pallas_call(
        body, name=name, grid=(rows // TM,),
        out_shape=[jax.ShapeDtypeStruct((rows, width), BF16), jax.ShapeDtypeStruct((lat_rows, d), F32),
                   jax.ShapeDtypeStruct((8, d), F32)],
        in_specs=in_specs,
        out_specs=[pl.BlockSpec((TM, width), full), pl.BlockSpec((TM, d), lat), pl.BlockSpec((8, d), fixed)],
        compiler_params=_cparams(dimension_semantics=("arbitrary",)),
    )(ctx0, x0, dz2, modc, modx, pre1, w_t, *pieces)


def _transposed_lhs_matmul(x_ref, dy_ref, o_ref, xt_ref):
    @pl.when(pl.program_id(1) == 0)
    def _():
        xt_ref[...] = x_ref[...].T

    o_ref[...] = jnp.dot(xt_ref[...], dy_ref[...], preferred_element_type=F32)


def _w_in_grad(dp, h1, n_cols, name):
    rows, d = h1.shape
    n_main = OFF_LR // HW
    lr0 = _w_in_row(OFF_LR)

    def body(x_ref, xlr_ref, h_ref, o_hbm, xt_ref, acc_ref, sem):
        i = pl.program_id(0)

        def main_copy(step):
            row = jnp.where(step < 9, step * HW, step * HW + 2 * RANK)
            return pltpu.make_async_copy(acc_ref, o_hbm.at[pl.ds(pl.multiple_of(row, 8), HW), :], sem)

        lr_copy = pltpu.make_async_copy(acc_ref.at[0:2 * RANK, :], o_hbm.at[lr0:lr0 + 2 * RANK, :], sem)

        @pl.when(i < n_main)
        def _():
            xt_ref[...] = x_ref[...].T

        @pl.when(i > 0)
        def _():
            main_copy(i - 1).wait()

        @pl.when(i < n_main)
        def _():
            acc_ref[...] = jnp.dot(xt_ref[...], h_ref[...], preferred_element_type=F32)
            main_copy(i).start()

        @pl.when(i == n_main)
        def _():
            xt_ref[0:128, :] = xlr_ref[...].T
            acc_ref[0:128, :] = jnp.dot(xt_ref[0:128, :], h_ref[...], preferred_element_type=F32)
            lr_copy.start()
            lr_copy.wait()

    return pl.pallas_call(
        body, name=name, grid=(n_main + 1,),
        out_shape=jax.ShapeDtypeStruct((n_cols, d), F32),
        in_specs=[pl.BlockSpec((rows, HW), lambda i: (0, jnp.minimum(i, n_main - 1))),
                  pl.BlockSpec((rows, 128), lambda i: (0, OFF_LR // 128)),
                  pl.BlockSpec((rows, d), lambda i: (0, 0))],
        out_specs=ANY_SPEC,
        scratch_shapes=[pltpu.VMEM((HW, rows), BF16), pltpu.VMEM((HW, d), F32), pltpu.SemaphoreType.DMA],
        compiler_params=_cparams(dimension_semantics=("arbitrary",)),
    )(dp, dp, h1)


def _weight_grad(xs, dy, name, tk=None, tn=512, k_first=0, k_tiles=None):
    rows = dy.shape[0]
    n = dy.shape[1]
    tn_ = min(tn, n)
    tk_ = xs.shape[1] if tk is None else tk
    k_tiles = xs.shape[1] // tk_ if k_tiles is None else k_tiles
    k = k_tiles * tk_

    return pl.pallas_call(
        functools.partial(_transposed_lhs_matmul), name=name, grid=(k_tiles, n // tn_),
        out_shape=jax.ShapeDtypeStruct((k, n), F32),
        in_specs=[pl.BlockSpec((rows, tk_), lambda i, j: (0, i + k_first)),
                  pl.BlockSpec((rows, tn_), lambda i, j: (0, j))],
        out_specs=pl.BlockSpec((tk_, tn_), lambda i, j: (i, j)),
        scratch_shapes=[pltpu.VMEM((tk_, rows), BF16)],
        compiler_params=_cparams(dimension_semantics=("parallel", "arbitrary")),
    )(xs, dy)


def _running_sum(x, fw):
    c = x.shape[0]
    row = lax.broadcasted_iota(jnp.int32, (c, 1), 0)
    s = 1
    while s < c:
        if fw:
            x = x + jnp.where(row >= s, pltpu.roll(x, s, axis=0), 0.0)
        else:
            x = x + jnp.where(row < c - s, pltpu.roll(x, c - s, axis=0), 0.0)
        s *= 2
    return x


def _chunk_terms(q, k, g, fw):
    c = CHUNK
    r = lax.broadcasted_iota(jnp.int32, (c, c), 0)
    s = lax.broadcasted_iota(jnp.int32, (c, c), 1)
    causal = (s <= r) if fw else (s >= r)
    causal_t = (s >= r) if fw else (s <= r)
    cum = _running_sum(g, fw)
    row = lax.broadcasted_iota(jnp.int32, (c, 1), 0)
    pos = row if fw else (c - 1 - row)
    starts = [None]
    for j in range(1, NSUB):
        rj = SUB * j - 1 if fw else c - SUB * j
        starts.append(cum[rj:rj + 1, :])
    in_blk = [(pos >= SUB * j) & (pos < SUB * (j + 1)) for j in range(NSUB)]
    e = [jnp.exp(cum)]
    for j in range(1, NSUB):
        e.append(jnp.exp(jnp.where(pos >= SUB * j, cum - starts[j], -1e30)))
    own = jnp.zeros_like(cum)
    for j in range(1, NSUB):
        own = own + jnp.where(in_blk[j], starts[j], 0.0)
    kscale = jnp.exp(own - cum)
    rend = c - 1 if fw else 0
    cend = cum[rend:rend + 1, :]
    tail = jnp.exp(cend - cum)
    qcat = jnp.concatenate([q * e[j] for j in range(NSUB)], axis=1).astype(BF16)
    kt = k * kscale
    km = jnp.concatenate([jnp.where(in_blk[j], kt, 0.0) for j in range(NSUB)], axis=1).astype(BF16)
    return dict(causal=causal, causal_t=causal_t, e=e, in_blk=in_blk, kscale=kscale, cend=cend, tail=tail,
                qcat=qcat, km=km)


def _chunk_fwd(q, k, v, g, st0, fw):
    t = _chunk_terms(q, k, g, fw)
    a = jnp.where(t["causal"], _dot_nt(t["qcat"], t["km"]), 0.0)
    o = _dot(a, v) + _dot_nt(t["qcat"][:, 0:HD], st0)
    st1 = st0 * jnp.exp(t["cend"]) + _dot_tn(v, k * t["tail"])
    return o, st1


def _chunk_bwd(q, k, v, g, st0, do, dst1, fw):
    t = _chunk_terms(q, k, g, fw)
    qcat, km, e = t["qcat"], t["km"], t["e"]
    a_t = jnp.where(t["causal_t"], _dot_nt(km, qcat), 0.0)
    ktail = k * t["tail"]
    dv = _dot(a_t, do) + _dot_nt(ktail, dst1)
    da = jnp.where(t["causal"], _dot_nt(do, v), 0.0)
    da_t = jnp.where(t["causal_t"], _dot_nt(v, do), 0.0)
    dqcat = _dot(da, km)
    dq_inter = e[0] * _dot(do, st0)
    dq = dq_inter
    for j in range(NSUB):
        dq = dq + e[j] * dqcat[:, j * HD:(j + 1) * HD]
    dkm = _dot(da_t, qcat)
    dkt = jnp.zeros_like(k)
    for j in range(NSUB):
        dkt = dkt + jnp.where(t["in_blk"][j], dkm[:, j * HD:(j + 1) * HD], 0.0)
    dk_inter = _dot(v, dst1) * t["tail"]
    dk = dkt * t["kscale"] + dk_inter
    dcum = q * dq_inter - k * dk_inter
    for j in range(NSUB):
        sl = slice(j * HD, (j + 1) * HD)
        dcum = dcum + qcat[:, sl].astype(F32) * dqcat[:, sl] - km[:, sl].astype(F32) * dkm[:, sl]
    ecend = jnp.exp(t["cend"])
    end = ecend * _colsum(st0 * dst1) + _colsum(k * dk_inter)
    dg = _running_sum(dcum, not fw) + end
    dst0 = dst1 * ecend + _dot_tn(do, q * e[0])
    return dq, dk, dv, dg, dst0


def _running_sums(xs, fws):
    c = xs[0].shape[0]
    row = lax.broadcasted_iota(jnp.int32, (c, 1), 0)
    s = 1
    while s < c:
        xs = [x + (jnp.where(row >= s, pltpu.roll(x, s, axis=0), 0.0) if fw else
                   jnp.where(row < c - s, pltpu.roll(x, c - s, axis=0), 0.0)) for x, fw in zip(xs, fws)]
        s *= 2
    return xs


def _chunks_terms(qs, ks, gs, fws):
    c = CHUNK
    n = len(qs)
    r = lax.broadcasted_iota(jnp.int32, (c, c), 0)
    s = lax.broadcasted_iota(jnp.int32, (c, c), 1)
    row = lax.broadcasted_iota(jnp.int32, (c, 1), 0)
    per_dir = {}
    for fw in set(fws):
        pos = row if fw else (c - 1 - row)
        per_dir[fw] = dict(
            causal=(s <= r) if fw else (s >= r), causal_t=(s >= r) if fw else (s <= r), pos=pos,
            in_blk=[(pos >= SUB * j) & (pos < SUB * (j + 1)) for j in range(NSUB)],
            start_row=[None] + [SUB * j - 1 if fw else c - SUB * j for j in range(1, NSUB)],
            rend=c - 1 if fw else 0)
    dirs = [per_dir[fw] for fw in fws]
    cums = _running_sums(gs, fws)
    starts = [[None] + [cum[d["start_row"][j]:d["start_row"][j] + 1, :] for j in range(1, NSUB)]
              for cum, d in zip(cums, dirs)]
    es = [[jnp.exp(cum) for cum in cums]]
    for j in range(1, NSUB):
        es.append([jnp.exp(jnp.where(d["pos"] >= SUB * j, cum - st[j], -1e30)) for cum, st, d in zip(cums, starts, dirs)])
    owns = [sum(jnp.where(d["in_blk"][j], st[j], 0.0) for j in range(1, NSUB)) for st, d in zip(starts, dirs)]
    kscales = [jnp.exp(own - cum) for own, cum in zip(owns, cums)]
    cends = [cum[d["rend"]:d["rend"] + 1, :] for cum, d in zip(cums, dirs)]
    tails = [jnp.exp(cend - cum) for cend, cum in zip(cends, cums)]
    qcats = [jnp.concatenate([q * es[j][i] for j in range(NSUB)], axis=1).astype(BF16) for i, q in enumerate(qs)]
    kts = [k * ksc for k, ksc in zip(ks, kscales)]
    kms = [jnp.concatenate([jnp.where(d["in_blk"][j], kt, 0.0) for j in range(NSUB)], axis=1).astype(BF16)
           for kt, d in zip(kts, dirs)]
    e_by_lane = [[es[j][i] for j in range(NSUB)] for i in range(n)]
    return dict(dirs=dirs, e=e_by_lane, kscale=kscales, cend=cends, tail=tails, qcat=qcats, km=kms)


def _chunks_fwd(qs, ks, vs, gs, st0s, fws):
    t = _chunks_terms(qs, ks, gs, fws)
    scores = [_dot_nt(qc, km) for qc, km in zip(t["qcat"], t["km"])]
    a = [jnp.where(d["causal"], sc, 0.0) for sc, d in zip(scores, t["dirs"])]
    inter = [_dot_nt(qc[:, 0:HD], st0) for qc, st0 in zip(t["qcat"], st0s)]
    intra = [_dot(a_, v) for a_, v in zip(a, vs)]
    os_ = [x + y for x, y in zip(intra, inter)]
    upd = [_dot_tn(v, k * tl) for v, k, tl in zip(vs, ks, t["tail"])]
    st1s = [st0 * jnp.exp(ce) + u for st0, ce, u in zip(st0s, t["cend"], upd)]
    return os_, st1s


def _chunks_bwd(qs, ks, vs, gs, st0s, dos, dst1s, fws):
    n = len(qs)
    t = _chunks_terms(qs, ks, gs, fws)
    qcat, km, e, dirs = t["qcat"], t["km"], t["e"], t["dirs"]
    a_t = [jnp.where(d["causal_t"], _dot_nt(km_, qc), 0.0) for km_, qc, d in zip(km, qcat, dirs)]
    ktail = [k * tl for k, tl in zip(ks, t["tail"])]
    dv_a = [_dot(at, do) for at, do in zip(a_t, dos)]
    dv_b = [_dot_nt(kt, ds) for kt, ds in zip(ktail, dst1s)]
    dv = [x + y for x, y in zip(dv_a, dv_b)]
    da = [jnp.where(d["causal"], _dot_nt(do, v), 0.0) for do, v, d in zip(dos, vs, dirs)]
    da_t = [jnp.where(d["causal_t"], _dot_nt(v, do), 0.0) for do, v, d in zip(dos, vs, dirs)]
    dqcat = [_dot(da_, km_) for da_, km_ in zip(da, km)]
    dq_inter = [e[i][0] * _dot(dos[i], st0s[i]) for i in range(n)]
    dkm = [_dot(dat, qc) for dat, qc in zip(da_t, qcat)]
    dk_inter = [_dot(v, ds) * tl for v, ds, tl in zip(vs, dst1s, t["tail"])]
    dq = [dq_inter[i] + sum(e[i][j] * dqcat[i][:, j * HD:(j + 1) * HD] for j in range(NSUB)) for i in range(n)]
    dkt = [sum(jnp.where(dirs[i]["in_blk"][j], dkm[i][:, j * HD:(j + 1) * HD], 0.0) for j in range(NSUB))
           for i in range(n)]
    dk = [dkt[i] * t["kscale"][i] + dk_inter[i] for i in range(n)]
    dcum = [qs[i] * dq_inter[i] - ks[i] * dk_inter[i]
            + sum(qcat[i][:, j * HD:(j + 1) * HD].astype(F32) * dqcat[i][:, j * HD:(j + 1) * HD]
                  - km[i][:, j * HD:(j + 1) * HD].astype(F32) * dkm[i][:, j * HD:(j + 1) * HD] for j in range(NSUB))
            for i in range(n)]
    ecend = [jnp.exp(ce) for ce in t["cend"]]
    end = [ecend[i] * _colsum(st0s[i] * dst1s[i]) + _colsum(ks[i] * dk_inter[i]) for i in range(n)]
    sums = _running_sums(dcum, [not fw for fw in fws])
    dg = [sm + en for sm, en in zip(sums, end)]
    upd = [_dot_tn(dos[i], qs[i] * e[i][0]) for i in range(n)]
    dst0 = [dst1s[i] * ecend[i] + upd[i] for i in range(n)]
    return dq, dk, dv, dg, dst0


def _chunk_index(step, n_ctx_chunks, n_chunks, fw):
    if fw:
        return step
    return jnp.where(step < n_ctx_chunks, n_ctx_chunks - 1 - step, n_chunks - 1 + n_ctx_chunks - step)


def _hg_inputs(hq, hf, lbv, d_idx, sl):
    lb = _sigmoid(lbv[d_idx:d_idx + 1, sl] - lbv[2 + d_idx:3 + d_idx, sl])
    sg = _sigmoid(hf)
    f = lb + (1.0 - lb) * sg
    return _silu(hq), 1.0 - f, jnp.log(f), f, sg, lb


def _scan_fwd(p, side, n_ctx_chunks, fw, branch, name):
    rows = p.shape[0]
    n_chunks = rows // CHUNK
    d_idx = 0 if fw else 1
    hg = branch == "hg"
    cols = (C_HQ, C_HI, C_HF_FW + d_idx) if hg else (C_GQ, C_GK, C_GV)

    def body(*refs):
        if hg:
            a_ref, b_ref, c_ref, lb_ref, o_ref, st_ref, state = refs
        else:
            a_ref, b_ref, c_ref, lr_ref, wgk_ref, bgk_ref, o_ref, st_ref, state = refs
            logits = _dot(lr_ref[...], wgk_ref[...]) + bgk_ref[...]
            g_all = _log_sigmoid(logits) * (1.0 / GATE_NORM)

        @pl.when(pl.program_id(0) == 0)
        def _():
            state[...] = jnp.zeros_like(state)

        for h in range(NH):
            sl = slice(h * HD, (h + 1) * HD)
            if hg:
                q, k, g, _, _, _ = _hg_inputs(a_ref[:, sl], c_ref[:, sl], lb_ref[...], d_idx, sl)
                v = b_ref[:, sl]
            else:
                q, k, v, g = a_ref[:, sl] * (HD ** -0.5), b_ref[:, sl], c_ref[:, sl], g_all[:, sl]
            st0 = state[h]
            st_ref[0, h] = st0
            o, st1 = _chunk_fwd(q, k, v, g, st0, fw)
            o_ref[:, sl] = o
            state[h] = st1

    def cmap(blk):
        return pl.BlockSpec((CHUNK, HW), lambda j: (_chunk_index(j, n_ctx_chunks, n_chunks, fw), blk))

    fixed = lambda j: (0, 0)
    in_specs = [cmap(cols[0]), cmap(cols[1]), cmap(cols[2])]
    if hg:
        in_specs += [pl.BlockSpec((4, HW), fixed)]
        args = (p, p, p, side)
    else:
        in_specs += [pl.BlockSpec((CHUNK, 128), lambda j: (_chunk_index(j, n_ctx_chunks, n_chunks, fw), OFF_LR // 128)),
                     pl.BlockSpec((128, HW), fixed), pl.BlockSpec((1, HW), fixed)]
        args = (p, p, p, p, side[0], side[1])
    return pl.pallas_call(
        body, name=name, grid=(n_chunks,),
        out_shape=[jax.ShapeDtypeStruct((rows, HW), F32), jax.ShapeDtypeStruct((n_chunks, NH, HD, HD), F32)],
        in_specs=in_specs,
        out_specs=[pl.BlockSpec((CHUNK, HW), lambda j: (_chunk_index(j, n_ctx_chunks, n_chunks, fw), 0)),
                   pl.BlockSpec((1, NH, HD, HD), lambda j: (_chunk_index(j, n_ctx_chunks, n_chunks, fw), 0, 0, 0))],
        scratch_shapes=[pltpu.VMEM((NH, HD, HD), F32)],
        compiler_params=_cparams(dimension_semantics=("arbitrary",)),
    )(*args)


def _scan_fwd_both(p, side, n_ctx_chunks, branch, name):
    rows = p.shape[0]
    n_chunks = rows // CHUNK
    hg = branch == "hg"
    n_in = 4 if hg else 6

    def body(*refs):
        ins, outs, state = refs[:2 * n_in], refs[2 * n_in:2 * n_in + 4], refs[-1]

        @pl.when(pl.program_id(0) == 0)
        def _():
            state[...] = jnp.zeros_like(state)

        lanes, where = [], []
        for di, fw in enumerate((True, False)):
            r = ins[di * n_in:(di + 1) * n_in]
            o_ref, st_ref = outs[2 * di], outs[2 * di + 1]
            if hg:
                a_ref, b_ref, c_ref, lb_ref = r
            else:
                a_ref, b_ref, c_ref, lr_ref, wgk_ref, bgk_ref = r
                logits = _dot(lr_ref[...], wgk_ref[...]) + bgk_ref[...]
                g_all = _log_sigmoid(logits) * (1.0 / GATE_NORM)
            for h in range(NH):
                sl = slice(h * HD, (h + 1) * HD)
                if hg:
                    q, k, g, _, _, _ = _hg_inputs(a_ref[:, sl], c_ref[:, sl], lb_ref[...], di, sl)
                    v = b_ref[:, sl]
                else:
                    q, k, v, g = a_ref[:, sl] * (HD ** -0.5), b_ref[:, sl], c_ref[:, sl], g_all[:, sl]
                lanes.append((q, k, v, g, state[di, h], fw))
                where.append((di, h, sl, o_ref, st_ref))
        qs, ks, vs, gs, st0s, fws = (list(col) for col in zip(*lanes))
        os_, st1s = _chunks_fwd(qs, ks, vs, gs, st0s, fws)
        for (di, h, sl, o_ref, st_ref), st0, o, st1 in zip(where, st0s, os_, st1s):
            st_ref[0, h] = st0
            o_ref[:, sl] = o
            state[di, h] = st1

    fixed = lambda j: (0, 0)
    in_specs, args, out_specs = [], [], []
    for di, fw in enumerate((True, False)):
        chunk = functools.partial(_chunk_index, n_ctx_chunks=n_ctx_chunks, n_chunks=n_chunks, fw=fw)

        def cmap(blk, width=HW, chunk=chunk):
            return pl.BlockSpec((CHUNK, width), lambda j: (chunk(j), blk))

        if hg:
            in_specs += [cmap(C_HQ), cmap(C_HI), cmap(C_HF_FW + di), pl.BlockSpec((4, HW), fixed)]
            args += [p, p, p, side]
        else:
            in_specs += [cmap(C_GQ), cmap(C_GK), cmap(C_GV), cmap(OFF_LR // 128, 128),
                         pl.BlockSpec((128, HW), fixed), pl.BlockSpec((1, HW), fixed)]
            args += [p, p, p, p, side[di][0], side[di][1]]
        out_specs += [cmap(0), pl.BlockSpec((1, NH, HD, HD), lambda j, chunk=chunk: (chunk(j), 0, 0, 0))]
    return pl.pallas_call(
        body, name=name, grid=(n_chunks,),
        out_shape=[jax.ShapeDtypeStruct((rows, HW), F32), jax.ShapeDtypeStruct((n_chunks, NH, HD, HD), F32)] * 2,
        in_specs=in_specs, out_specs=out_specs,
        scratch_shapes=[pltpu.VMEM((2, NH, HD, HD), F32)],
        compiler_params=_cparams(dimension_semantics=("arbitrary",)),
    )(*args)


def _scan_bwd_both(p, side, states, d_o, n_ctx_chunks, branch, name):
    rows = p.shape[0]
    n_chunks = rows // CHUNK
    hg = branch == "hg"
    n_in = 6 if hg else 8
    n_out = 4 if hg else 6

    def body(*refs):
        ins, outs, dstate = refs[:2 * n_in], refs[2 * n_in:2 * n_in + 2 * n_out], refs[-1]
        first = pl.program_id(0) == 0

        @pl.when(first)
        def _():
            dstate[...] = jnp.zeros_like(dstate)

        lanes, where, extra, ctx = [], [], [], []
        for di, fw in enumerate((True, False)):
            r, w = ins[di * n_in:(di + 1) * n_in], outs[di * n_out:(di + 1) * n_out]
            if hg:
                a_ref, b_ref, c_ref, lb_ref, st_ref, do_ref = r
                da_ref, db_ref, dc_ref, dlb_ref = w
                acc_refs = (dlb_ref,)
            else:
                a_ref, b_ref, c_ref, lr_ref, wgk_ref, bgk_ref, st_ref, do_ref = r
                da_ref, db_ref, dc_ref, dlr_ref, dwgk_ref, dbias_ref = w
                acc_refs = (dwgk_ref, dbias_ref)
                lr = lr_ref[...]
                logits = _dot(lr, wgk_ref[...]) + bgk_ref[...]
                g_all = _log_sigmoid(logits) * (1.0 / GATE_NORM)

            @pl.when(first)
            def _(acc_refs=acc_refs):
                for ref in acc_refs:
                    ref[...] = jnp.zeros_like(ref)

            for h in range(NH):
                sl = slice(h * HD, (h + 1) * HD)
                if hg:
                    hq, hf = a_ref[:, sl], c_ref[:, sl]
                    q, k, g, f, sg, lb = _hg_inputs(hq, hf, lb_ref[...], di, sl)
                    v = b_ref[:, sl]
                    extra.append((hq, f, sg, lb))
                else:
                    q, k, v, g = a_ref[:, sl] * (HD ** -0.5), b_ref[:, sl], c_ref[:, sl], g_all[:, sl]
                    extra.append(None)
                lanes.append((q, k, v, g, st_ref[0, h], do_ref[:, sl], dstate[di, h], fw))
                where.append((di, h, sl))
            ctx.append((w, None if hg else (lr, logits, wgk_ref)))

        qs, ks, vs, gs, st0s, dos, dst1s, fws = (list(col) for col in zip(*lanes))
        dqs, dks, dvs, dgs, dst0s = _chunks_bwd(qs, ks, vs, gs, st0s, dos, dst1s, fws)
        dg_parts = {0: [], 1: []}
        for (di, h, sl), ex, dq, dk, dv, dg, dst0 in zip(where, extra, dqs, dks, dvs, dgs, dst0s):
            dstate[di, h] = dst0
            w = ctx[di][0]
            if hg:
                hq, f, sg, lb = ex
                da_ref, db_ref, dc_ref, dlb_ref = w
                da_ref[:, sl] = dq * _dsilu(hq)
                db_ref[:, sl] = dv
                df = dg / f - dk
                dc_ref[:, sl] = df * (1.0 - lb) * sg * (1.0 - sg)
                dlb_ref[0:1, sl] += _colsum(df * (1.0 - sg))
            else:
                da_ref, db_ref, dc_ref = w[:3]
                da_ref[:, sl] = dq * (HD ** -0.5)
                db_ref[:, sl] = dk
                dc_ref[:, sl] = dv
                dg_parts[di].append(dg)
        if not hg:
            for di in range(2):
                dlr_ref, dwgk_ref, dbias_ref = ctx[di][0][3:]
                lr, logits, wgk_ref = ctx[di][1]
                dlogits = jnp.concatenate(dg_parts[di], axis=1) * (1.0 / GATE_NORM) * (1.0 - _sigmoid(logits))
                dlr_ref[...] = _dot_nt(dlogits, wgk_ref[...])
                dwgk_ref[...] += _dot_tn(lr, dlogits)
                dbias_ref[0:1, :] += _colsum(dlogits)

    fixed = lambda j: (0, 0)
    big = jax.ShapeDtypeStruct((rows, HW), F32)
    in_specs, args, out_shape, out_specs = [], [], [], []
    for di, fw in enumerate((True, False)):
        def chunk_of(j, fw=fw):
            return _chunk_index(n_chunks - 1 - j, n_ctx_chunks, n_chunks, fw)

        def cmap(blk, width=HW, chunk_of=chunk_of):
            return pl.BlockSpec((CHUNK, width), lambda j: (chunk_of(j), blk))

        st_spec = pl.BlockSpec((1, NH, HD, HD), lambda j, chunk_of=chunk_of: (chunk_of(j), 0, 0, 0))
        if hg:
            in_specs += [cmap(C_HQ), cmap(C_HI), cmap(C_HF_FW + di), pl.BlockSpec((4, HW), fixed), st_spec, cmap(0)]
            args += [p, p, p, side, states[di], d_o]
            out_shape += [big, big, big, jax.ShapeDtypeStruct((8, HW), F32)]
            out_specs += [cmap(0), cmap(0), cmap(0), pl.BlockSpec((8, HW), fixed)]
        else:
            in_specs += [cmap(C_GQ), cmap(C_GK), cmap(C_GV), cmap(OFF_LR // 128, 128),
                         pl.BlockSpec((128, HW), fixed), pl.BlockSpec((1, HW), fixed), st_spec, cmap(0)]
            args += [p, p, p, p, side[di][0], side[di][1], states[di], d_o]
            out_shape += [big, big, big, jax.ShapeDtypeStruct((rows, 128), F32),
                          jax.ShapeDtypeStruct((128, HW), F32), jax.ShapeDtypeStruct((8, HW), F32)]
            out_specs += [cmap(0), cmap(0), cmap(0), cmap(0, 128), pl.BlockSpec((128, HW), fixed),
                          pl.BlockSpec((8, HW), fixed)]
    return pl.pallas_call(
        body, name=name, grid=(n_chunks,), out_shape=out_shape, in_specs=in_specs, out_specs=out_specs,
        scratch_shapes=[pltpu.VMEM((2, NH, HD, HD), F32)],
        compiler_params=_cparams(dimension_semantics=("arbitrary",)),
    )(*args)


def _scan_bwd(p, side, states, d_o, n_ctx_chunks, fw, branch, name):
    rows = p.shape[0]
    n_chunks = rows // CHUNK
    d_idx = 0 if fw else 1
    hg = branch == "hg"
    cols = (C_HQ, C_HI, C_HF_FW + d_idx) if hg else (C_GQ, C_GK, C_GV)

    def body(*refs):
        if hg:
            a_ref, b_ref, c_ref, lb_ref, st_ref, do_ref, da_ref, db_ref, dc_ref, dlb_ref, dstate = refs
        else:
            (a_ref, b_ref, c_ref, lr_ref, wgk_ref, bgk_ref, st_ref, do_ref, da_ref, db_ref, dc_ref, dlr_ref,
             dwgk_ref, dbias_ref, dstate) = refs
            lr = lr_ref[...]
            logits = _dot(lr, wgk_ref[...]) + bgk_ref[...]
            g_all = _log_sigmoid(logits) * (1.0 / GATE_NORM)

        @pl.when(pl.program_id(0) == 0)
        def _():
            dstate[...] = jnp.zeros_like(dstate)
            if hg:
                dlb_ref[...] = jnp.zeros_like(dlb_ref)
            else:
                dwgk_ref[...] = jnp.zeros_like(dwgk_ref)
                dbias_ref[...] = jnp.zeros_like(dbias_ref)

        dg_parts = []
        for h in range(NH):
            sl = slice(h * HD, (h + 1) * HD)
            if hg:
                hq, hf = a_ref[:, sl], c_ref[:, sl]
                q, k, g, f, sg, lb = _hg_inputs(hq, hf, lb_ref[...], d_idx, sl)
                v = b_ref[:, sl]
            else:
                q, k, v, g = a_ref[:, sl] * (HD ** -0.5), b_ref[:, sl], c_ref[:, sl], g_all[:, sl]
            dq, dk, dv, dg, dst0 = _chunk_bwd(q, k, v, g, st_ref[0, h], do_ref[:, sl], dstate[h], fw)
            dstate[h] = dst0
            if hg:
                da_ref[:, sl] = dq * _dsilu(hq)
                db_ref[:, sl] = dv
                df = dg / f - dk
                dc_ref[:, sl] = df * (1.0 - lb) * sg * (1.0 - sg)
                dlb_ref[0:1, sl] += _colsum(df * (1.0 - sg))
            else:
                da_ref[:, sl] = dq * (HD ** -0.5)
                db_ref[:, sl] = dk
                dc_ref[:, sl] = dv
                dg_parts.append(dg)
        if not hg:
            dlogits = jnp.concatenate(dg_parts, axis=1) * (1.0 / GATE_NORM) * (1.0 - _sigmoid(logits))
            dlr_ref[...] = _dot_nt(dlogits, wgk_ref[...])
            dwgk_ref[...] += _dot_tn(lr, dlogits)
            dbias_ref[0:1, :] += _colsum(dlogits)

    def chunk_of(j):
        return _chunk_index(n_chunks - 1 - j, n_ctx_chunks, n_chunks, fw)

    def cmap(blk, width=HW):
        return pl.BlockSpec((CHUNK, width), lambda j: (chunk_of(j), blk))

    fixed = lambda j: (0, 0)
    st_spec = pl.BlockSpec((1, NH, HD, HD), lambda j: (chunk_of(j), 0, 0, 0))
    big = jax.ShapeDtypeStruct((rows, HW), F32)
    if hg:
        in_specs = [cmap(cols[0]), cmap(cols[1]), cmap(cols[2]), pl.BlockSpec((4, HW), fixed), st_spec, cmap(0)]
        args = (p, p, p, side, states, d_o)
        out_shape = [big, big, big, jax.ShapeDtypeStruct((8, HW), F32)]
        out_specs = [cmap(0), cmap(0), cmap(0), pl.BlockSpec((8, HW), fixed)]
    else:
        in_specs = [cmap(cols[0]), cmap(cols[1]), cmap(cols[2]), cmap(OFF_LR // 128, 128),
                    pl.BlockSpec((128, HW), fixed), pl.BlockSpec((1, HW), fixed), st_spec, cmap(0)]
        args = (p, p, p, p, side[0], side[1], states, d_o)
        out_shape = [big, big, big, jax.ShapeDtypeStruct((rows, 128), F32), jax.ShapeDtypeStruct((128, HW), F32),
                     jax.ShapeDtypeStruct((8, HW), F32)]
        out_specs = [cmap(0), cmap(0), cmap(0), cmap(0, 128), pl.BlockSpec((128, HW), fixed),
                     pl.BlockSpec((8, HW), fixed)]
    return pl.pallas_call(
        body, name=name, grid=(n_chunks,), out_shape=out_shape, in_specs=in_specs, out_specs=out_specs,
        scratch_shapes=[pltpu.VMEM((NH, HD, HD), F32)],
        compiler_params=_cparams(dimension_semantics=("arbitrary",)),
    )(*args)


SMALL_ROWS = 56
ROWS_MOD_X = (0, 1, 8, 16, 17, 18)
ROWS_MOD_C = (2, 3)
ROW_PRE1, ROW_POST1, ROW_ONORM, ROW_PRE2, ROW_POST2, ROW_LB, ROW_BGK, ROW_WGK = 4, 9, 10, 19, 20, 24, 32, 40
ROW_LOSS = 21


def _reduce_small(gathered, lb_full, name):
    _, _, d = gathered.shape

    def body(g_ref, lb_ref, sum_ref, dmod_ref, dbmod_ref, dlb_ref):
        total = g_ref[0]
        for b in range(1, N_DEV):
            total = total + g_ref[b]
        sum_ref[...] = total
        dmod_ref[...] = jnp.zeros_like(dmod_ref)
        for m in range(N_MOD):
            col = slice(m * d, (m + 1) * d)
            acc = jnp.zeros((1, d), F32)
            for b in range(N_DEV):
                row = g_ref[b, ROWS_MOD_X[m]:ROWS_MOD_X[m] + 1, :]
                dmod_ref[b:b + 1, col] = row
                acc = acc + row
            if m < 2:
                ctx_row = total[ROWS_MOD_C[m]:ROWS_MOD_C[m] + 1, :]
                dmod_ref[8:9, col] = ctx_row
                acc = acc + ctx_row
            dbmod_ref[:, col] = acc
        lbv = lb_ref[...]
        for dd in range(2):
            lb = _sigmoid(lbv[dd:dd + 1, :] - lbv[2 + dd:3 + dd, :])
            gl = total[ROW_LB:ROW_LB + 1, dd * HW:(dd + 1) * HW] * lb * (1.0 - lb)
            dlb_ref[dd:dd + 1, :] = gl
            dlb_ref[2 + dd:3 + dd, :] = -gl

    return pl.pallas_call(
        body, name=name,
        out_shape=[jax.ShapeDtypeStruct((SMALL_ROWS, d), F32), jax.ShapeDtypeStruct((16, N_MOD * d), F32),
                   jax.ShapeDtypeStruct((1, N_MOD * d), F32), jax.ShapeDtypeStruct((4, HW), F32)],
        in_specs=[VMEM_SPEC] * 2, out_specs=[VMEM_SPEC] * 4, compiler_params=_cparams(),
    )(gathered, lb_full)


def _c_ctx_grad(gathered, c_ctx_row, name):
    def body(g_ref, c_ref, o_ref):
        acc = g_ref[0, 0:1, :]
        for chip in range(1, N_CHIP):
            acc = acc + g_ref[2 * chip, 0:1, :]
        o_ref[...] = acc * _dsilu(c_ref[...])

    return pl.pallas_call(
        body, name=name, out_shape=jax.ShapeDtypeStruct(c_ctx_row.shape, F32),
        in_specs=[VMEM_SPEC] * 2, out_specs=VMEM_SPEC, compiler_params=_cparams(),
    )(gathered, c_ctx_row)


def _relayout_w_in(w):
    pad = jnp.zeros((w.shape[0], 128 - 2 * RANK), w.dtype)
    return jnp.concatenate([w[:, :9 * HW], w[:, 9 * HW + 2 * RANK:], w[:, 9 * HW:9 * HW + 2 * RANK], pad], axis=1)


def _relayout_w_in_rows(wt):
    pad = jnp.zeros((128 - 2 * RANK, wt.shape[1]), wt.dtype)
    return jnp.concatenate([wt[:9 * HW], wt[9 * HW + 2 * RANK:], wt[9 * HW:9 * HW + 2 * RANK], pad], axis=0)


def _w_in_grad_rows(g_main, g_lr):
    return jnp.concatenate([g_main[:9 * HW], g_lr[:2 * RANK], g_main[9 * HW:]], axis=0)


def _w_in_grad_blocks(g_main, g_lr, n_blocks):
    lr0 = 9 * HW
    n = (g_main.shape[1] + 2 * RANK) // n_blocks

    def cols(lo, hi):
        out = []
        if lo < lr0:
            out.append(g_main[:, lo:min(hi, lr0)])
        if hi > lr0 and lo < lr0 + 2 * RANK:
            out.append(g_lr[:, max(lo, lr0) - lr0:min(hi, lr0 + 2 * RANK) - lr0])
        if hi > lr0 + 2 * RANK:
            out.append(g_main[:, max(lo, lr0 + 2 * RANK) - 2 * RANK:hi - 2 * RANK])
        return out

    return jnp.stack([jnp.concatenate(cols(j * n, (j + 1) * n), axis=1) for j in range(n_blocks)])


def _blocked(full, n_blocks):
    k, n = full.shape
    return full.reshape(k, n_blocks, n // n_blocks).transpose(1, 0, 2)


def _unblocked(blocks):
    nb, k, n = blocks.shape
    return blocks.transpose(1, 0, 2).reshape(k, nb * n)


def _sample_front(x0, ctx0, modc, modx, norm_pre1, lb_full, gla_side, w_in_r):
    ctx_len = ctx0.shape[0]
    n_ctx_tiles = ctx_len // TM
    n_ctx_chunks = ctx_len // CHUNK
    h1, p = _in_projection(ctx0, x0, modc, modx, norm_pre1, w_in_r, n_ctx_tiles, "in_projection")
    o_hg_fw, st_hg_fw, o_hg_bw, st_hg_bw = _scan_fwd_both(p, lb_full, n_ctx_chunks, "hg", "scan_hg")
    o_gla_fw, st_gla_fw, o_gla_bw, st_gla_bw = _scan_fwd_both(p, gla_side, n_ctx_chunks, "gla", "scan_gla")
    return dict(h1=h1, p=p, o_list=[o_hg_fw, o_hg_bw, o_gla_fw, o_gla_bw],
                states=[st_hg_fw, st_hg_bw, st_gla_fw, st_gla_bw])


def _sample_back(reduce, front, x0, ctx0, target0, modc, modx, norm_pre1, norms, onorms, lb_full, gla_side, w_in_r,
                 wbh, wbg, wout, ffn_weights):
    seq, d = x0.shape
    ctx_len = ctx0.shape[0]
    n_ctx_tiles = ctx_len // TM
    n_tiles = (ctx_len + seq) // TM
    n_ctx_chunks = ctx_len // CHUNK
    h1, p, o_list = front["h1"], front["p"], front["o_list"]
    st_hg_fw, st_hg_bw, st_gla_fw, st_gla_bw = front["states"]
    z2, y1, merged, og_hg, og_gla = _mixer_tail_fwd(x0, p, o_list, modx, norms, onorms, wbh, wbg, wout, n_ctx_tiles,
                                                    "mixer_tail")
    wg, wu, wd = ffn_weights([z2])
    loss_part, dz2, h2, a_act, du, dv, dy2, stat_ffn = _ffn_fwd_bwd(z2, modx, norms, wg, wu, wd, target0, "ffn")
    dff = wg.shape[0]
    tok = reduce("ffn", [_weight_grad(du, h2, "grad_w_ff_gate", tk=dff // 2, tn=d),
                         _weight_grad(dv, h2, "grad_w_ff_up", tk=dff // 2, tn=d),
                         _weight_grad(a_act, dy2, "grad_w_ff_down", tk=dff // 2)])

    (d_ohg, d_ogla, d_hgate, d_ggate, d_ghg, d_ggla, dy1, db_hg, db_gla, stat_mix) = _mixer_tail_bwd(
        x0, p, o_list, dz2, y1, modx + tok, norms, onorms, wbh, wbg, wout, n_ctx_tiles, n_tiles, "mixer_tail_bwd")
    tok = reduce("mix", [_weight_grad(og_hg, db_hg, "grad_w_br_hg"), _weight_grad(og_gla, db_gla, "grad_w_br_gla"),
                         _weight_grad(merged, dy1, "grad_w_out")])
    tok = tok + reduce("push_ffn", [dy1])
    gla_b = [(wgk, bias + tok) for wgk, bias in gla_side]
    (dgq_f, dgk_f, dgv_f, dlr_f, dwgk_f, dbgk_f, dgq_b, dgk_b, dgv_b, dlr_b, dwgk_b, dbgk_b) = _scan_bwd_both(
        p, gla_b, (st_gla_fw, st_gla_bw), d_ogla, n_ctx_chunks, "gla", "scan_gla_bwd")
    lb_b = lb_full + reduce("push_mix", [dbgk_f])
    (dhq_f, dhi_f, dhf_f, dlb_f, dhq_b, dhi_b, dhf_b, dlb_b) = _scan_bwd_both(
        p, lb_b, (st_hg_fw, st_hg_bw), d_ohg, n_ctx_chunks, "hg", "scan_hg_bwd")
    pieces = [dhq_f, dhq_b, dhi_f, dhi_b, dhf_f, dhf_b, d_hgate, dgq_f, dgq_b, dgk_f, dgk_b, dgv_f, dgv_b, d_ggate,
              d_ghg, d_ggla, dlr_f, dlr_b]
    dp, grad_x, stat_in = _in_projection_bwd(ctx0, x0, dz2, modc, modx, norm_pre1, w_in_r, pieces, n_ctx_tiles,
                                             "in_projection_bwd")

    tok = reduce("in", [_w_in_grad(dp, h1, w_in_r.shape[0], "grad_w_in")])
    reduce("small", dict(stat_in=stat_in + tok, stat_mix=stat_mix, stat_ffn=stat_ffn, dlb=(dlb_f, dlb_b),
                         dwgk=(dwgk_f, dwgk_b), dbgk=(dbgk_f, dbgk_b)))
    reduce("push_in", [])
    return dict(
        loss_part=loss_part, grad_x=grad_x, stat_in=stat_in, stat_mix=stat_mix, stat_ffn=stat_ffn,
        dlb=(dlb_f, dlb_b), dwgk=(dwgk_f, dwgk_b), dbgk=(dbgk_f, dbgk_b))


def kernel(x, c, ctx, c_ctx, w_mod, b_mod, norm_pre1, norm_post1, norm_pre2, norm_post2, w_in, hg_lb, hg_onorm, gla_w_gk, gla_b_gk, gla_onorm, w_br_hg, w_br_gla, w_out, w_ff_gate, w_ff_up, w_ff_down, loss_target, m_c_ctx, m_w_mod, m_b_mod, m_norm_pre1, m_norm_post1, m_norm_pre2, m_norm_post2, m_w_in, m_hg_lb, m_hg_onorm, m_gla_w_gk, m_gla_b_gk, m_gla_onorm, m_w_br_hg, m_w_br_gla, m_w_out, m_w_ff_gate, m_w_ff_up, m_w_ff_down, v_c_ctx, v_w_mod, v_b_mod, v_norm_pre1, v_norm_post1, v_norm_pre2, v_norm_post2, v_w_in, v_hg_lb, v_hg_onorm, v_gla_w_gk, v_gla_b_gk, v_gla_onorm, v_w_br_hg, v_w_br_gla, v_w_out, v_w_ff_gate, v_w_ff_up, v_w_ff_down):
    seq, d = x.shape[1], x.shape[2]
    ctx_len = ctx.shape[1]
    assert seq % TM == 0 and ctx_len % TM == 0 and d == 2 * HW
    ax, ay, ac = lax.axis_index("x"), lax.axis_index("y"), lax.axis_index("c")
    chip = 2 * ax + ay
    dev = 2 * chip + ac
    c_arr = jnp.reshape(ac, (1,)).astype(jnp.int32)
    chip_arr = jnp.reshape(chip, (1,)).astype(jnp.int32)
    transposed = ("w_in", "w_ff_gate", "w_ff_up")
    view = lambda a, nm: a[0].T if nm in transposed else a[0]

    sems_in, lands_in, token_in0 = _blocks_start([_cast_into_blocks(chip_arr, view(w_in, "w_in"), "cast_w_in")],
                                                 "gather_w_in_start")

    nc = d // 128
    pad8 = lambda a: jnp.pad(a, ((0, -a.shape[0] % 8), (0, 0)))
    small1 = jnp.concatenate([c.reshape(nc, 128) + token_in0[0, 0], pad8(hg_lb.reshape(4, 128)),
                              gla_w_gk.reshape(2 * RANK, 128), pad8(gla_b_gk.reshape(2, 128))], axis=0)
    blocks = [_cast_into_blocks(chip_arr, view(w_, nm), "cast_" + nm) for w_, nm in (
        (w_br_hg, "w_br_hg"), (w_br_gla, "w_br_gla"), (w_out, "w_out"), (w_ff_gate, "w_ff_gate"),
        (w_ff_up, "w_ff_up"), (w_ff_down, "w_ff_down"))]
    got1 = _allgather8(small1, "gather_small_params", after=blocks)
    c_all = got1[:, :nc, :].reshape(N_DEV, d)
    per_chip = got1[0::2]
    lb_full = per_chip[:, nc:nc + 4, :].transpose(1, 0, 2).reshape(4, HW)
    wgk_full = per_chip[:, nc + 8:nc + 8 + 2 * RANK, :].transpose(1, 0, 2).reshape(2, RANK, HW)
    bgk_full = per_chip[:, nc + 8 + 2 * RANK:nc + 10 + 2 * RANK, :].transpose(1, 0, 2).reshape(2, HW)
    wgk_pad = [jnp.zeros((128, HW), F32).at[dd * RANK:(dd + 1) * RANK].set(wgk_full[dd]) for dd in range(2)]
    bgk = [bgk_full[dd:dd + 1] for dd in range(2)]

    n_mod_cols = w_mod.shape[2]
    cond = jnp.concatenate([c_all, pad8(c_ctx.reshape(1, d))], axis=0)
    b_cols = lax.dynamic_slice(b_mod, (0, chip * n_mod_cols), (1, n_mod_cols))
    mod_part = _mod_forward(cond, w_mod[0], b_cols, "mod_forward")
    mod_got = _allgather8(mod_part, "gather_mod")
    mod_all = mod_got[0::2].transpose(1, 0, 2).reshape(16, N_CHIP * n_mod_cols)
    modx = pad8(lax.dynamic_slice(mod_all, (dev, 0), (1, N_MOD * d)).reshape(N_MOD, d))
    modc = pad8(mod_all[8].reshape(N_MOD, d))

    lands_in = _blocks_wait(sems_in, lands_in, [mod_got], "gather_w_in_wait")
    gathered_in = _blocks_finish(lands_in, "gather_w_in_finish")
    sems, lands, token = _blocks_start(blocks, "gather_rest_start", after=[gathered_in[0]])
    w_in_r = gathered_in[0].reshape(-1, d)

    norms = jnp.concatenate([norm_pre1, norm_post1, norm_pre2, norm_post2, jnp.zeros((4, d), F32)], axis=0)
    onorms = jnp.zeros((8, d), F32).at[0, :HD].set(hg_onorm[0]).at[1, :HD].set(gla_onorm[0])
    gla_side = [(wgk_pad[dd], bgk[dd]) for dd in range(2)]
    modx = modx + token[0, 0]
    front = _sample_front(x[0], ctx[0], modc, modx, norm_pre1, lb_full, gla_side, w_in_r)
    lands = _blocks_wait(sems, lands, front["o_list"], "gather_rest_wait")
    gathered = _blocks_finish(lands[:3], "gather_mix_finish")
    wbh, wbg = _unblocked(gathered[0]), _unblocked(gathered[1])
    wout = gathered[2].reshape(d, d)
    ffn_sems, ffn_lands, ffn_token = _forward_start(lands[3:], "gather_ffn_forward_start")
    onorms = onorms + ffn_token[0, 0]

    def ffn_weights(after):
        got = _forward_wait(ffn_sems, ffn_lands, after, "gather_ffn_forward_wait")
        return tuple(g.reshape(-1, d) for g in got)

    dff = w_ff_down.shape[1] * N_CHIP
    groups = {"ffn": ["w_ff_gate", "w_ff_up", "w_ff_down"], "mix": ["w_br_hg", "w_br_gla", "w_out"], "in": ["w_in"]}
    row_sharded = {"w_out": d // N_CHIP, "w_ff_down": dff // N_CHIP, "w_ff_gate": dff // N_CHIP,
                   "w_ff_up": dff // N_CHIP, "w_in": w_in.shape[2]}
    in_flight, to_sibling = {}, {}

    small = {}

    def reduce_small(stats):
        small2 = jnp.concatenate([
            stats["stat_in"], stats["stat_mix"], stats["stat_ffn"],
            jnp.concatenate(stats["dlb"], axis=1), jnp.concatenate(stats["dbgk"], axis=1),
            jnp.concatenate([stats["dwgk"][0][0:RANK], stats["dwgk"][1][RANK:2 * RANK]], axis=1)], axis=0)
        assert small2.shape[0] == SMALL_ROWS
        got2 = _allgather8(small2, "gather_small_grads")
        total, dmod_all, g_b_mod, g_lb_full = _reduce_small(got2, lb_full, "reduce_small")
        dmod_cols = lax.dynamic_slice(dmod_all, (0, chip * n_mod_cols), (16, n_mod_cols))
        g_w_mod, cctx_part = _mod_backward(cond, w_mod[0], dmod_cols, "mod_backward")
        got3 = _allgather8(cctx_part, "gather_c_ctx_grad")
        g_c_ctx = _c_ctx_grad(got3, c_ctx.reshape(1, d), "c_ctx_grad")
        small.update(total=total, g_b_mod=g_b_mod, g_lb_full=g_lb_full, g_w_mod=g_w_mod, g_c_ctx=g_c_ctx)

    def reduce(group, grads):
        if group == "small":
            return reduce_small(grads)
        if group.startswith("push_"):
            return push(group[5:], grads)
        nms = groups[group]
        full = [g.reshape(N_CHIP, row_sharded[nm], d) if nm in row_sharded else _blocked(g, N_CHIP)
                for g, nm in zip(grads, nms)]
        sems_, full, lands_, token_ = _send_half_start(full, "grads_to_sibling_start_" + group)
        to_sibling[group] = (sems_, full, lands_)
        return token_[0, 0]

    def push(group, after):
        nms = groups[group]
        sems_, full, lands_ = to_sibling[group]
        if group == "in":
            after = list(after) + [small["g_c_ctx"], small["total"]]
        full, from_sibling = _send_half_wait(sems_, full, lands_, after, "grads_to_sibling_wait_" + group)
        pairs = [_pair_sum(c_arr, f, r_, "pair_sum_" + nm) for f, r_, nm in zip(full, from_sibling, nms)]
        after = [small["g_c_ctx"], small["total"]] if group == "in" else []
        sems_, pairs, lands_, token_ = _scatter_start(pairs, "grads_to_owner_start_" + group, after)
        in_flight[group] = (sems_, pairs, lands_, token_)
        return token_[0, 0]

    r = _sample_back(reduce, front, x[0], ctx[0], loss_target[0], modc, modx, norm_pre1, norms, onorms, lb_full,
                     gla_side, w_in_r, wbh, wbg, wout, ffn_weights)
    loss_part, grad_x, stat_in, stat_mix, stat_ffn = (r[k] for k in ("loss_part", "grad_x", "stat_in", "stat_mix",
                                                                     "stat_ffn"))
    (dlb_f, dlb_b), (dwgk_f, dwgk_b), (dbgk_f, dbgk_b) = r["dlb"], r["dwgk"], r["dbgk"]

    weights = dict(w_in=(w_in, m_w_in, v_w_in), w_br_hg=(w_br_hg, m_w_br_hg, v_w_br_hg),
                   w_br_gla=(w_br_gla, m_w_br_gla, v_w_br_gla), w_out=(w_out, m_w_out, v_w_out),
                   w_ff_gate=(w_ff_gate, m_w_ff_gate, v_w_ff_gate), w_ff_up=(w_ff_up, m_w_ff_up, v_w_ff_up),
                   w_ff_down=(w_ff_down, m_w_ff_down, v_w_ff_down))
    names = ["w_in", "w_br_hg", "w_br_gla", "w_out", "w_ff_gate", "w_ff_up", "w_ff_down"]
    big = {}

    swapping = {}

    def sum_and_swap(group, after):
        sems_, pairs, lands_, _ = in_flight[group]
        pairs, lands_ = _scatter_wait(sems_, pairs, lands_, after, "grads_to_owner_wait_" + group)
        own_half = [_sum_owner(chip_arr, pr, g, "chip_sum_" + nm) for pr, g, nm in zip(pairs, lands_, groups[group])]
        swapping[group] = _swap_start(own_half, "halves_to_sibling_start_" + group)
        return own_half[-1]

    def update(group, after):
        sems_, own_half, lands_ = swapping[group]
        own_half, other_half = _swap_wait(sems_, own_half, lands_, after, "halves_to_sibling_wait_" + group)
        done = []
        for nm, own, oth in zip(groups[group], own_half, other_half):
            w_, m_, v_ = (view(a, nm) for a in weights[nm])
            res = _adamw_halves(c_arr, own, oth, w_, m_, v_, "adamw_" + nm)
            big[nm] = [r_.T[None] if nm in transposed else r_[None] for r_ in res]
            done.append(res[1])
        return done

    token_in = in_flight["in"][3]
    summed_ffn = sum_and_swap("ffn", [token_in])
    summed_mix = sum_and_swap("mix", [summed_ffn])

    total, g_b_mod, g_lb_full, g_w_mod, g_c_ctx = (small[k] for k in ("total", "g_b_mod", "g_lb_full", "g_w_mod",
                                                                      "g_c_ctx"))
    g_pre1, g_post1, g_pre2, g_post2 = (total[r_:r_ + 1] for r_ in (ROW_PRE1, ROW_POST1, ROW_PRE2, ROW_POST2))
    g_hg_on, g_gla_on = total[ROW_ONORM:ROW_ONORM + 1, 0:HD], total[ROW_ONORM:ROW_ONORM + 1, HD:2 * HD]
    n_lb = hg_lb.shape[2]
    g_hg_lb = lax.dynamic_slice(g_lb_full, (0, chip * n_lb), (4, n_lb))
    g_bgk = lax.dynamic_slice(total[ROW_BGK:ROW_BGK + 1].reshape(2, HW), (0, chip * n_lb), (2, n_lb))
    g_wgk_full = total[ROW_WGK:ROW_WGK + RANK].reshape(RANK, 2, HW).transpose(1, 0, 2).reshape(2 * RANK, HW)
    g_wgk = lax.dynamic_slice(g_wgk_full, (0, chip * n_lb), (2 * RANK, n_lb))

    small_items = [
        (g_c_ctx, c_ctx.reshape(1, d), m_c_ctx.reshape(1, d), v_c_ctx.reshape(1, d)),
        (g_b_mod, b_mod, m_b_mod, v_b_mod),
        (g_pre1, norm_pre1, m_norm_pre1, v_norm_pre1),
        (g_post1, norm_post1, m_norm_post1, v_norm_post1),
        (g_pre2, norm_pre2, m_norm_pre2, v_norm_pre2),
        (g_post2, norm_post2, m_norm_post2, v_norm_post2),
        (g_hg_lb, hg_lb.reshape(4, n_lb), m_hg_lb.reshape(4, n_lb), v_hg_lb.reshape(4, n_lb)),
        (g_hg_on, hg_onorm, m_hg_onorm, v_hg_onorm),
        (g_wgk, gla_w_gk.reshape(2 * RANK, n_lb), m_gla_w_gk.reshape(2 * RANK, n_lb), v_gla_w_gk.reshape(2 * RANK, n_lb)),
        (g_bgk, gla_b_gk.reshape(2, n_lb), m_gla_b_gk.reshape(2, n_lb), v_gla_b_gk.reshape(2, n_lb)),
        (g_gla_on, gla_onorm, m_gla_onorm, v_gla_onorm),
    ]
    small_res = _adamw_whole(small_items, "adamw_small")
    mod_res = _adamw_tiled(g_w_mod, w_mod[0], m_w_mod[0], v_w_mod[0], "adamw_w_mod")
    done_ffn = update("ffn", [summed_mix, mod_res[0], small_res[0][0]])
    done_mix = update("mix", done_ffn)
    update("in", [sum_and_swap("in", done_mix)])

    loss = total[ROW_LOSS, 0]

    shapes = dict(c_ctx=c_ctx.shape, b_mod=b_mod.shape, norm_pre1=norm_pre1.shape, norm_post1=norm_post1.shape,
                  norm_pre2=norm_pre2.shape, norm_post2=norm_post2.shape, hg_lb=hg_lb.shape, hg_onorm=hg_onorm.shape,
                  gla_w_gk=gla_w_gk.shape, gla_b_gk=gla_b_gk.shape, gla_onorm=gla_onorm.shape)
    small_names = ["c_ctx", "b_mod", "norm_pre1", "norm_post1", "norm_pre2", "norm_post2", "hg_lb", "hg_onorm",
                   "gla_w_gk", "gla_b_gk", "gla_onorm"]
    grads, deltas, new_m, new_v = {}, {}, {}, {}
    for nm, item, res in zip(small_names, small_items, small_res):
        grads[nm] = item[0].reshape(shapes[nm])
        deltas[nm], new_m[nm], new_v[nm] = (r.reshape(shapes[nm]) for r in res)
    grads["w_mod"] = g_w_mod[None]
    deltas["w_mod"], new_m["w_mod"], new_v["w_mod"] = (r[None] for r in mod_res)
    for nm in names:
        grads[nm], deltas[nm], new_m[nm], new_v[nm] = big[nm]
    order = ["c_ctx", "w_mod", "b_mod", "norm_pre1", "norm_post1", "norm_pre2", "norm_post2", "w_in", "hg_lb",
             "hg_onorm", "gla_w_gk", "gla_b_gk", "gla_onorm", "w_br_hg", "w_br_gla", "w_out", "w_ff_gate", "w_ff_up",
             "w_ff_down"]
    return (loss, grad_x[None], *[grads[n] for n in order], *[deltas[n] for n in order],
            *[new_m[n] for n in order], *[new_v[n] for n in order])


def _weight_grad_cols(xs, dy, n_cols, name, tn=512):
    rows = dy.shape[0]
    k = tk = xs.shape[1]

    return pl.pallas_call(
        functools.partial(_transposed_lhs_matmul), name=name, grid=(k // tk, n_cols // tn),
        out_shape=jax.ShapeDtypeStruct((k, n_cols), F32),
        in_specs=[pl.BlockSpec((rows, tk), lambda i, j: (0, i)), pl.BlockSpec((rows, tn), lambda i, j: (0, j))],
        out_specs=pl.BlockSpec((tk, tn), lambda i, j: (i, j)),
        scratch_shapes=[pltpu.VMEM((tk, rows), BF16)],
        compiler_params=_cparams(dimension_semantics=("parallel", "arbitrary")),
    )(xs, dy)
```

```python
import functools

import jax
import jax.numpy as jnp
from jax import lax
from jax.experimental import pallas as pl
from jax.experimental.pallas import tpu as pltpu

F32 = jnp.float32
BF16 = jnp.bfloat16
HIGHEST = lax.Precision.HIGHEST
MESH = pl.DeviceIdType.MESH

EPS = 1e-6
CHUNK = 64
SUB = 16
NSUB = CHUNK // SUB
NH = 4
HD = 128
HW = NH * HD
RANK = 16
GATE_NORM = 16.0
N_MOD = 6
TM = 256
TM_FFN = 128
N_DEV = 8
N_CHIP = 4
VMEM_LIMIT = 56 * 1024 * 1024

ADAM_LR = 0.001
ADAM_B1 = 0.9
ADAM_B2 = 0.999
ADAM_EPS = 1e-08
ADAM_WD = 0.01
ADAM_STEP = 10

VMEM_SPEC = pl.BlockSpec(memory_space=pltpu.VMEM)
ANY_SPEC = pl.BlockSpec(memory_space=pl.ANY)
HBM_SPEC = pl.BlockSpec(memory_space=pltpu.HBM)
SEM_SPEC = pl.BlockSpec(memory_space=pltpu.SEMAPHORE)
EFFECT = pltpu.SideEffectType.DATAFLOW_SIDE_EFFECTING


def _cparams(**kw):
    return pltpu.CompilerParams(vmem_limit_bytes=VMEM_LIMIT, **kw)


def _dot(a, b):
    return jnp.dot(a.astype(BF16), b.astype(BF16), preferred_element_type=F32)


def _dot_nt(a, b):
    return lax.dot_general(a.astype(BF16), b.astype(BF16), (((1,), (1,)), ((), ())), preferred_element_type=F32)


def _dot_tn(a, b):
    return lax.dot_general(a.astype(BF16), b.astype(BF16), (((0,), (0,)), ((), ())), preferred_element_type=F32)


def _sigmoid(x):
    return 1.0 / (1.0 + jnp.exp(-x))


def _silu(x):
    return x * _sigmoid(x)


def _dsilu(x):
    s = _sigmoid(x)
    return s * (1.0 + x * (1.0 - s))


def _log_sigmoid(x):
    return jnp.minimum(x, 0.0) - jnp.log(1.0 + jnp.exp(-jnp.abs(x)))


def _colsum(a):
    return jnp.sum(a, axis=0, keepdims=True)


def _rms(a):
    r = lax.rsqrt(jnp.mean(a * a, axis=-1, keepdims=True) + EPS)
    return a * r, r


def _rms_bwd(dn, n, r):
    return r * (dn - n * jnp.mean(dn * n, axis=-1, keepdims=True))


def _place():
    x, y, c = lax.axis_index("x"), lax.axis_index("y"), lax.axis_index("c")
    chips = [(1 - x, y), (x, 1 - y), (1 - x, 1 - y)]
    return x, y, c, chips


def _allgather8(v, name, after=()):
    rows, cols = v.shape
    n_after = len(after)

    def body(x_ref, *rest):
        out_ref, send_sems, recv_sems, local_sem = rest[n_after:]
        x, y, c, chips = _place()
        me, sibling = (x, y, c), (x, y, 1 - c)

        def blk(px, py, pc):
            return out_ref.at[4 * px + 2 * py + pc]

        def copy(k, block, to, src=None):
            return pltpu.make_async_remote_copy(
                src_ref=blk(*block) if src is None else src, dst_ref=blk(*block),
                send_sem=send_sems.at[k], recv_sem=recv_sems.at[k], device_id=to, device_id_type=MESH)

        mine = pltpu.make_async_copy(x_ref, blk(*me), local_sem)
        mine.start()
        first = [copy(0, me, sibling, src=x_ref)]
        first += [copy(1 + j, me, (*chip, c), src=x_ref) for j, chip in enumerate(chips)]
        for cp in first:
            cp.start()
        passed = [copy(4 + j, (*chip, c), sibling) for j, chip in enumerate(chips)]
        for j, chip in enumerate(chips):
            copy(1 + j, (*chip, c), me).wait_recv()
            passed[j].start()
        copy(0, sibling, me).wait_recv()
        for j, chip in enumerate(chips):
            copy(4 + j, (*chip, 1 - c), me).wait_recv()
        for cp in first + passed:
            cp.wait_send()
        mine.wait()

    return pl.pallas_call(
        body, name=name,
        out_shape=jax.ShapeDtypeStruct((N_DEV, rows, cols), v.dtype),
        in_specs=[VMEM_SPEC] + [ANY_SPEC] * n_after, out_specs=VMEM_SPEC,
        scratch_shapes=[pltpu.SemaphoreType.DMA((7,)), pltpu.SemaphoreType.DMA((7,)), pltpu.SemaphoreType.DMA],
    )(v, *after)


def _cast_into_blocks(chip_arr, w, name):
    rows, cols = w.shape
    tr = _row_tile(rows, 16, 256)

    def body(chip_ref, w_ref, o_ref):
        o_ref[0] = w_ref[...].astype(BF16)

    return pl.pallas_call(
        body, name=name,
        grid_spec=pltpu.PrefetchScalarGridSpec(
            num_scalar_prefetch=1, grid=(rows // tr,),
            in_specs=[pl.BlockSpec((tr, cols), lambda i, chip_ref: (i, 0))],
            out_specs=pl.BlockSpec((1, tr, cols), lambda i, chip_ref: (chip_ref[0], i, 0))),
        out_shape=jax.ShapeDtypeStruct((N_CHIP, rows, cols), BF16),
        compiler_params=_cparams(dimension_semantics=("parallel",)),
    )(chip_arr, w)


def _halved_by_rows(shape):
    return (shape[1] // 2) % 16 == 0


def _half_of(ref, pc, block=None):
    lead = slice(None) if block is None else block
    if _halved_by_rows(ref.shape):
        h = ref.shape[1] // 2
        return ref.at[lead, pl.ds(pl.multiple_of(pc * h, 16), h), :]
    h = ref.shape[2] // 2
    return ref.at[lead, :, pl.ds(pl.multiple_of(pc * h, 128), h)]


def _half_shape(shape):
    return (shape[0], shape[1] // 2, shape[2]) if _halved_by_rows(shape) else (shape[0], shape[1], shape[2] // 2)


def _half_rows(ref, chip_id, pc):
    return _half_of(ref, pc, chip_id)


def _gather_blocks(lands, name, after=()):
    n = len(lands)
    n_in = n + len(after)

    def body(*refs):
        outs = refs[n_in:n_in + n]
        send_sems, recv_sems = refs[n_in + n:]
        x, y, c, chips = _place()
        me_chip = 2 * x + y
        sibling = (x, y, 1 - c)

        def copy(k, j, chip_id, pc, to):
            return pltpu.make_async_remote_copy(
                src_ref=_half_rows(outs[k], chip_id, pc), dst_ref=_half_rows(outs[k], chip_id, pc),
                send_sem=send_sems.at[k, j], recv_sem=recv_sems.at[k, j], device_id=to, device_id_type=MESH)

        started = []
        for k in range(n):
            for j, chip in enumerate(chips):
                cp = copy(k, j, me_chip, c, (*chip, c))
                cp.start()
                started.append(cp)
        for k in range(n):
            for j, (px, py) in enumerate(chips):
                copy(k, j, 2 * px + py, c, sibling).wait_recv()
                cp = copy(k, 3 + j, 2 * px + py, c, sibling)
                cp.start()
                started.append(cp)
        for k in range(n):
            for j, (px, py) in enumerate(chips):
                copy(k, 3 + j, 2 * px + py, 1 - c, sibling).wait_recv()
        for cp in started:
            cp.wait_send()

    return pl.pallas_call(
        body, name=name,
        out_shape=[jax.ShapeDtypeStruct(l.shape, l.dtype) for l in lands],
        in_specs=[ANY_SPEC] * n_in, out_specs=[ANY_SPEC] * n,
        input_output_aliases={i: i for i in range(n)},
        scratch_shapes=[pltpu.SemaphoreType.DMA((n, 6)), pltpu.SemaphoreType.DMA((n, 6))],
    )(*lands, *after)


def _hbm(a):
    return pltpu.with_memory_space_constraint(a, pltpu.HBM)


def _blocks_start(lands, name, after=()):
    n = len(lands)
    n_sem = 3 * n
    first = n + len(after)

    def body(*refs):
        lnd = refs[:n]
        send_sems, recv_sems = refs[first:first + n_sem], refs[first + n_sem:first + 2 * n_sem]
        token = refs[-1]
        x, y, c, chips = _place()
        me_chip = 2 * x + y
        for k in range(n):
            for j, chip in enumerate(chips):
                pltpu.make_async_remote_copy(
                    src_ref=_half_rows(lnd[k], me_chip, c), dst_ref=_half_rows(lnd[k], me_chip, c),
                    send_sem=send_sems[3 * k + j], recv_sem=recv_sems[3 * k + j],
                    device_id=(*chip, c), device_id_type=MESH).start()
        token[...] = jnp.zeros_like(token)

    out = pl.pallas_call(
        body, name=name,
        out_shape=(*[pltpu.SemaphoreType.DMA(())] * (2 * n_sem),
                   *[pltpu.HBM(l.shape, l.dtype) for l in lands],
                   jax.ShapeDtypeStruct((8, 128), F32)),
        in_specs=[HBM_SPEC] * n + [ANY_SPEC] * len(after),
        out_specs=(*[SEM_SPEC] * (2 * n_sem), *[HBM_SPEC] * n, VMEM_SPEC),
        input_output_aliases={i: 2 * n_sem + i for i in range(n)},
        compiler_params=pltpu.CompilerParams(has_side_effects=EFFECT),
    )(*[_hbm(l) for l in lands], *after)
    return list(out[:2 * n_sem]), list(out[2 * n_sem:2 * n_sem + n]), out[-1]


def _blocks_wait(sems, lands, after, name):
    n = len(lands)
    n_sem = 3 * n

    def body(*refs):
        lnd = refs[:n]
        s_sems, r_sems = refs[n:n + n_sem], refs[n + n_sem:n + 2 * n_sem]
        x, y, c, chips = _place()
        me_chip = 2 * x + y
        for k in range(n):
            for j, (px, py) in enumerate(chips):
                cp = pltpu.make_async_remote_copy(
                    src_ref=_half_rows(lnd[k], me_chip, c), dst_ref=_half_rows(lnd[k], 2 * px + py, c),
                    send_sem=s_sems[3 * k + j], recv_sem=r_sems[3 * k + j],
                    device_id=(px, py, c), device_id_type=MESH)
                cp.wait_send()
                cp.wait_recv()

    out = pl.pallas_call(
        body, name=name,
        out_shape=tuple(pltpu.HBM(l.shape, l.dtype) for l in lands),
        in_specs=[HBM_SPEC] * n + [SEM_SPEC] * (2 * n_sem) + [ANY_SPEC] * len(after),
        out_specs=[HBM_SPEC] * n,
        input_output_aliases={i: i for i in range(n)},
        compiler_params=pltpu.CompilerParams(has_side_effects=EFFECT),
    )(*lands, *sems, *after)
    return list(out)


def _forward_start(lands, name):
    n = len(lands)
    n_sem = 3 * n

    def body(*refs):
        lnd = refs[:n]
        send_sems, recv_sems = refs[n:n + n_sem], refs[n + n_sem:n + 2 * n_sem]
        x, y, c, chips = _place()
        for k in range(n):
            for j, (px, py) in enumerate(chips):
                pltpu.make_async_remote_copy(
                    src_ref=_half_rows(lnd[k], 2 * px + py, c), dst_ref=_half_rows(lnd[k], 2 * px + py, c),
                    send_sem=send_sems[3 * k + j], recv_sem=recv_sems[3 * k + j],
                    device_id=(x, y, 1 - c), device_id_type=MESH).start()
        refs[-1][...] = jnp.zeros_like(refs[-1])

    out = pl.pallas_call(
        body, name=name,
        out_shape=(*[pltpu.SemaphoreType.DMA(())] * (2 * n_sem), *[pltpu.HBM(l.shape, l.dtype) for l in lands],
                   jax.ShapeDtypeStruct((8, 128), F32)),
        in_specs=[HBM_SPEC] * n,
        out_specs=(*[SEM_SPEC] * (2 * n_sem), *[HBM_SPEC] * n, VMEM_SPEC),
        input_output_aliases={i: 2 * n_sem + i for i in range(n)},
        compiler_params=pltpu.CompilerParams(has_side_effects=EFFECT),
    )(*[_hbm(l) for l in lands])
    return list(out[:2 * n_sem]), list(out[2 * n_sem:2 * n_sem + n]), out[-1]


def _forward_wait(sems, lands, after, name):
    n = len(lands)
    n_sem = 3 * n

    def body(*refs):
        lnd = refs[:n]
        s_sems, r_sems = refs[n:n + n_sem], refs[n + n_sem:n + 2 * n_sem]
        x, y, c, chips = _place()
        for k in range(n):
            for j, (px, py) in enumerate(chips):
                cp = pltpu.make_async_remote_copy(
                    src_ref=_half_rows(lnd[k], 2 * px + py, c), dst_ref=_half_rows(lnd[k], 2 * px + py, 1 - c),
                    send_sem=s_sems[3 * k + j], recv_sem=r_sems[3 * k + j],
                    device_id=(x, y, 1 - c), device_id_type=MESH)
                cp.wait_send()
                cp.wait_recv()

    out = pl.pallas_call(
        body, name=name,
        out_shape=tuple(pltpu.HBM(l.shape, l.dtype) for l in lands),
        in_specs=[HBM_SPEC] * n + [SEM_SPEC] * (2 * n_sem) + [ANY_SPEC] * len(after),
        out_specs=[HBM_SPEC] * n,
        input_output_aliases={i: i for i in range(n)},
        compiler_params=pltpu.CompilerParams(has_side_effects=EFFECT),
    )(*lands, *sems, *after)
    return list(out)


def _blocks_finish(lands, name):
    n = len(lands)

    def body(*refs):
        lnd = refs[n:2 * n]
        send_sems, recv_sems = refs[2 * n:]
        x, y, c, chips = _place()
        sibling = (x, y, 1 - c)

        def copy(k, j, chip_id, pc):
            return pltpu.make_async_remote_copy(
                src_ref=_half_rows(lnd[k], chip_id, pc), dst_ref=_half_rows(lnd[k], chip_id, pc),
                send_sem=send_sems.at[k, j], recv_sem=recv_sems.at[k, j], device_id=sibling, device_id_type=MESH)

        started = []
        for k in range(n):
            for j, (px, py) in enumerate(chips):
                cp = copy(k, j, 2 * px + py, c)
                cp.start()
                started.append(cp)
        for k in range(n):
            for j, (px, py) in enumerate(chips):
                copy(k, j, 2 * px + py, 1 - c).wait_recv()
        for cp in started:
            cp.wait_send()

    out = pl.pallas_call(
        body, name=name,
        out_shape=[jax.ShapeDtypeStruct(l.shape, l.dtype) for l in lands],
        in_specs=[ANY_SPEC] * n, out_specs=[ANY_SPEC] * n,
        input_output_aliases={i: i for i in range(n)},
        scratch_shapes=[pltpu.SemaphoreType.DMA((n, 3)), pltpu.SemaphoreType.DMA((n, 3))],
    )(*lands)
    return list(out)


def _gather_start(shards, name):
    n = len(shards)
    n_sem = 3 * n

    def body(*refs):
        ins, lands = refs[:n], refs[n:2 * n]
        send_sems, recv_sems = refs[2 * n:2 * n + n_sem], refs[2 * n + n_sem:2 * n + 2 * n_sem]
        token = refs[-1]
        x, y, c, chips = _place()
        me_chip = 2 * x + y
        for k in range(n):
            h = shards[k].shape[0] // 2
            rows = pl.ds(pl.multiple_of(c * h, 8), h)
            for j, chip in enumerate(chips):
                pltpu.make_async_remote_copy(
                    src_ref=ins[k].at[rows, :], dst_ref=lands[k].at[me_chip, rows, :],
                    send_sem=send_sems[3 * k + j], recv_sem=recv_sems[3 * k + j],
                    device_id=(*chip, c), device_id_type=MESH).start()
        token[...] = jnp.zeros_like(token)

    lands = [_hbm(lax.empty((N_CHIP,) + s.shape, s.dtype)) for s in shards]
    out = pl.pallas_call(
        body, name=name,
        out_shape=(*[pltpu.SemaphoreType.DMA(())] * (2 * n_sem),
                   *[pltpu.HBM(s.shape, s.dtype) for s in shards],
                   *[pltpu.HBM(l.shape, l.dtype) for l in lands],
                   jax.ShapeDtypeStruct((8, 128), F32)),
        in_specs=[HBM_SPEC] * (2 * n),
        out_specs=(*[SEM_SPEC] * (2 * n_sem), *[HBM_SPEC] * (2 * n), VMEM_SPEC),
        input_output_aliases={i: 2 * n_sem + i for i in range(2 * n)},
        compiler_params=pltpu.CompilerParams(has_side_effects=EFFECT),
    )(*[_hbm(s) for s in shards], *lands)
    sems = list(out[:2 * n_sem])
    return sems, list(out[2 * n_sem:2 * n_sem + n]), list(out[2 * n_sem + n:2 * n_sem + 2 * n]), out[-1]


def _gather_wait(sems, shards, lands, after, name):
    n = len(shards)
    n_sem = 3 * n

    def body(*refs):
        ins, lnd = refs[:n], refs[n:2 * n]
        s_sems, r_sems = refs[2 * n:2 * n + n_sem], refs[2 * n + n_sem:2 * n + 2 * n_sem]
        x, y, c, chips = _place()
        for k in range(n):
            h = shards[k].shape[0] // 2
            rows = pl.ds(pl.multiple_of(c * h, 8), h)
            for j, (px, py) in enumerate(chips):
                cp = pltpu.make_async_remote_copy(
                    src_ref=ins[k].at[rows, :], dst_ref=lnd[k].at[2 * px + py, rows, :],
                    send_sem=s_sems[3 * k + j], recv_sem=r_sems[3 * k + j],
                    device_id=(px, py, c), device_id_type=MESH)
                cp.wait_send()
                cp.wait_recv()

    out = pl.pallas_call(
        body, name=name,
        out_shape=(*[pltpu.HBM(s.shape, s.dtype) for s in shards], *[pltpu.HBM(l.shape, l.dtype) for l in lands]),
        in_specs=[HBM_SPEC] * (2 * n) + [SEM_SPEC] * (2 * n_sem) + [ANY_SPEC],
        out_specs=[HBM_SPEC] * (2 * n),
        input_output_aliases={i: i for i in range(2 * n)},
        compiler_params=pltpu.CompilerParams(has_side_effects=EFFECT),
    )(*shards, *lands, *sems, after)
    return list(out[:n]), list(out[n:])


def _gather_finish(shards, lands, name):
    n = len(shards)

    def body(*refs):
        ins, lnd = refs[:n], refs[2 * n:3 * n]
        send_sems, recv_sems, local_sems = refs[3 * n:]
        x, y, c, chips = _place()
        me_chip = 2 * x + y
        sibling = (x, y, 1 - c)

        def half(k, chip_id, pc):
            h = shards[k].shape[0] // 2
            return lnd[k].at[chip_id, pl.ds(pl.multiple_of(pc * h, 8), h), :]

        def copy(k, j, chip_id, pc):
            return pltpu.make_async_remote_copy(
                src_ref=half(k, chip_id, pc), dst_ref=half(k, chip_id, pc),
                send_sem=send_sems.at[k, j], recv_sem=recv_sems.at[k, j], device_id=sibling, device_id_type=MESH)

        locals_, started = [], []
        for k in range(n):
            cp = pltpu.make_async_copy(ins[k], lnd[k].at[me_chip], local_sems.at[k])
            cp.start()
            locals_.append(cp)
            for j, (px, py) in enumerate(chips):
                cp = copy(k, j, 2 * px + py, c)
                cp.start()
                started.append(cp)
        for k in range(n):
            for j, (px, py) in enumerate(chips):
                copy(k, j, 2 * px + py, 1 - c).wait_recv()
        for cp in started:
            cp.wait_send()
        for cp in locals_:
            cp.wait()

    out = pl.pallas_call(
        body, name=name,
        out_shape=[jax.ShapeDtypeStruct(l.shape, l.dtype) for l in lands],
        in_specs=[ANY_SPEC] * (2 * n), out_specs=[ANY_SPEC] * n,
        input_output_aliases={n + i: i for i in range(n)},
        scratch_shapes=[pltpu.SemaphoreType.DMA((n, 3)), pltpu.SemaphoreType.DMA((n, 3)),
                        pltpu.SemaphoreType.DMA((n,))],
    )(*shards, *lands)
    return list(out)


def _send_other_half(arrs, name):
    n = len(arrs)

    def body(*refs):
        ins, outs = refs[:n], refs[n:2 * n]
        send_sems, recv_sems = refs[2 * n:]
        x, y, c, _ = _place()
        cps = []
        for k in range(n):
            cp = pltpu.make_async_remote_copy(
                src_ref=_half_of(ins[k], 1 - c), dst_ref=outs[k],
                send_sem=send_sems.at[k], recv_sem=recv_sems.at[k], device_id=(x, y, 1 - c), device_id_type=MESH)
            cp.start()
            cps.append(cp)
        for cp in cps:
            cp.wait()

    return pl.pallas_call(
        body, name=name,
        out_shape=[jax.ShapeDtypeStruct(_half_shape(a.shape), a.dtype) for a in arrs],
        in_specs=[ANY_SPEC] * n, out_specs=[ANY_SPEC] * n,
        scratch_shapes=[pltpu.SemaphoreType.DMA((n,)), pltpu.SemaphoreType.DMA((n,))],
    )(*arrs)


def _send_half_start(arrs, name):
    n = len(arrs)

    def body(*refs):
        ins, lnd = refs[:n], refs[n:2 * n]
        send_sems, recv_sems = refs[2 * n:3 * n], refs[3 * n:4 * n]
        token = refs[-1]
        x, y, c, _ = _place()
        for k in range(n):
            pltpu.make_async_remote_copy(
                src_ref=_half_of(ins[k], 1 - c), dst_ref=lnd[k], send_sem=send_sems[k], recv_sem=recv_sems[k],
                device_id=(x, y, 1 - c), device_id_type=MESH).start()
        token[...] = jnp.zeros_like(token)

    lands = [_hbm(lax.empty(_half_shape(a.shape), a.dtype)) for a in arrs]
    out = pl.pallas_call(
        body, name=name,
        out_shape=(*[pltpu.SemaphoreType.DMA(())] * (2 * n), *[pltpu.HBM(a.shape, a.dtype) for a in arrs],
                   *[pltpu.HBM(l.shape, l.dtype) for l in lands], jax.ShapeDtypeStruct((8, 128), F32)),
        in_specs=[HBM_SPEC] * (2 * n),
        out_specs=(*[SEM_SPEC] * (2 * n), *[HBM_SPEC] * (2 * n), VMEM_SPEC),
        input_output_aliases={i: 2 * n + i for i in range(2 * n)},
        compiler_params=pltpu.CompilerParams(has_side_effects=EFFECT),
    )(*[_hbm(a) for a in arrs], *lands)
    return list(out[:2 * n]), list(out[2 * n:3 * n]), list(out[3 * n:4 * n]), out[-1]


def _send_half_wait(sems, arrs, lands, after, name):
    n = len(arrs)

    def body(*refs):
        ins, lnd = refs[:n], refs[n:2 * n]
        s_sems, r_sems = refs[2 * n:3 * n], refs[3 * n:4 * n]
        x, y, c, _ = _place()
        for k in range(n):
            cp = pltpu.make_async_remote_copy(
                src_ref=_half_of(ins[k], 1 - c), dst_ref=lnd[k], send_sem=s_sems[k], recv_sem=r_sems[k],
                device_id=(x, y, 1 - c), device_id_type=MESH)
            cp.wait_send()
            cp.wait_recv()

    out = pl.pallas_call(
        body, name=name,
        out_shape=tuple(pltpu.HBM(a.shape, a.dtype) for a in list(arrs) + list(lands)),
        in_specs=[HBM_SPEC] * (2 * n) + [SEM_SPEC] * (2 * n) + [ANY_SPEC] * len(after),
        out_specs=[HBM_SPEC] * (2 * n),
        input_output_aliases={i: i for i in range(2 * n)},
        compiler_params=pltpu.CompilerParams(has_side_effects=EFFECT),
    )(*arrs, *lands, *sems, *after)
    return list(out[:n]), list(out[n:])


def _blocks_to_owner(arrs, name):
    n = len(arrs)

    def body(*refs):
        ins, outs = refs[:n], refs[n:2 * n]
        send_sems, recv_sems, local_sems = refs[2 * n:]
        x, y, c, chips = _place()
        me_chip = 2 * x + y
        locals_, started = [], []
        for k in range(n):
            cp = pltpu.make_async_copy(ins[k].at[me_chip], outs[k].at[me_chip], local_sems.at[k])
            cp.start()
            locals_.append(cp)

        def copy(k, j, src_block, dst_slot, to):
            return pltpu.make_async_remote_copy(
                src_ref=ins[k].at[src_block], dst_ref=outs[k].at[dst_slot],
                send_sem=send_sems.at[k, j], recv_sem=recv_sems.at[k, j], device_id=to, device_id_type=MESH)

        for k in range(n):
            for j, (px, py) in enumerate(chips):
                cp = copy(k, j, 2 * px + py, me_chip, (px, py, c))
                cp.start()
                started.append(cp)
        for k in range(n):
            for j, (px, py) in enumerate(chips):
                copy(k, j, me_chip, 2 * px + py, (px, py, c)).wait_recv()
        for cp in started:
            cp.wait_send()
        for cp in locals_:
            cp.wait()

    return pl.pallas_call(
        body, name=name,
        out_shape=[jax.ShapeDtypeStruct(a.shape, a.dtype) for a in arrs],
        in_specs=[ANY_SPEC] * n, out_specs=[ANY_SPEC] * n,
        scratch_shapes=[pltpu.SemaphoreType.DMA((n, 3)), pltpu.SemaphoreType.DMA((n, 3)),
                        pltpu.SemaphoreType.DMA((n,))],
    )(*arrs)


def _scatter_blocks(arrs, name):
    n = len(arrs)

    def body(*refs):
        ins, outs = refs[:n], refs[n:2 * n]
        send_sems, recv_sems = refs[2 * n:]
        x, y, c, chips = _place()
        me_chip = 2 * x + y

        def copy(k, j, src_block, dst_slot, to):
            return pltpu.make_async_remote_copy(
                src_ref=ins[k].at[src_block], dst_ref=outs[k].at[dst_slot],
                send_sem=send_sems.at[k, j], recv_sem=recv_sems.at[k, j], device_id=to, device_id_type=MESH)

        started = []
        for k in range(n):
            for j, (px, py) in enumerate(chips):
                cp = copy(k, j, 2 * px + py, me_chip, (px, py, c))
                cp.start()
                started.append(cp)
        for k in range(n):
            for j, (px, py) in enumerate(chips):
                copy(k, j, me_chip, 2 * px + py, (px, py, c)).wait_recv()
        for cp in started:
            cp.wait_send()

    return pl.pallas_call(
        body, name=name,
        out_shape=[jax.ShapeDtypeStruct(a.shape, a.dtype) for a in arrs],
        in_specs=[ANY_SPEC] * n, out_specs=[ANY_SPEC] * n,
        scratch_shapes=[pltpu.SemaphoreType.DMA((n, 3)), pltpu.SemaphoreType.DMA((n, 3))],
    )(*arrs)


def _scatter_start(arrs, name, after=()):
    n = len(arrs)
    n_sem = 3 * n
    first = 2 * n + len(after)

    def body(*refs):
        ins, lnd = refs[:n], refs[n:2 * n]
        send_sems, recv_sems = refs[first:first + n_sem], refs[first + n_sem:first + 2 * n_sem]
        token = refs[-1]
        x, y, c, chips = _place()
        me_chip = 2 * x + y
        for k in range(n):
            for j, (px, py) in enumerate(chips):
                pltpu.make_async_remote_copy(
                    src_ref=ins[k].at[2 * px + py], dst_ref=lnd[k].at[me_chip],
                    send_sem=send_sems[3 * k + j], recv_sem=recv_sems[3 * k + j],
                    device_id=(px, py, c), device_id_type=MESH).start()
        token[...] = jnp.zeros_like(token)

    lands = [_hbm(lax.empty(a.shape, a.dtype)) for a in arrs]
    out = pl.pallas_call(
        body, name=name,
        out_shape=(*[pltpu.SemaphoreType.DMA(())] * (2 * n_sem),
                   *[pltpu.HBM(a.shape, a.dtype) for a in arrs], *[pltpu.HBM(a.shape, a.dtype) for a in arrs],
                   jax.ShapeDtypeStruct((8, 128), F32)),
        in_specs=[HBM_SPEC] * (2 * n) + [ANY_SPEC] * len(after),
        out_specs=(*[SEM_SPEC] * (2 * n_sem), *[HBM_SPEC] * (2 * n), VMEM_SPEC),
        input_output_aliases={i: 2 * n_sem + i for i in range(2 * n)},
        compiler_params=pltpu.CompilerParams(has_side_effects=EFFECT),
    )(*[_hbm(a) for a in arrs], *lands, *after)
    base = 2 * n_sem
    return list(out[:base]), list(out[base:base + n]), list(out[base + n:base + 2 * n]), out[-1]


def _scatter_wait(sems, arrs, lands, after, name):
    n = len(arrs)
    n_sem = 3 * n

    def body(*refs):
        ins, lnd = refs[:n], refs[n:2 * n]
        s_sems, r_sems = refs[2 * n:2 * n + n_sem], refs[2 * n + n_sem:2 * n + 2 * n_sem]
        x, y, c, chips = _place()
        for k in range(n):
            for j, (px, py) in enumerate(chips):
                cp = pltpu.make_async_remote_copy(
                    src_ref=ins[k].at[2 * px + py], dst_ref=lnd[k].at[2 * px + py],
                    send_sem=s_sems[3 * k + j], recv_sem=r_sems[3 * k + j],
                    device_id=(px, py, c), device_id_type=MESH)
                cp.wait_send()
                cp.wait_recv()

    out = pl.pallas_call(
        body, name=name,
        out_shape=tuple(pltpu.HBM(a.shape, a.dtype) for a in list(arrs) + list(lands)),
        in_specs=[HBM_SPEC] * (2 * n) + [SEM_SPEC] * (2 * n_sem) + [ANY_SPEC] * len(after),
        out_specs=[HBM_SPEC] * (2 * n),
        input_output_aliases={i: i for i in range(2 * n)},
        compiler_params=pltpu.CompilerParams(has_side_effects=EFFECT),
    )(*arrs, *lands, *sems, *after)
    return list(out[:n]), list(out[n:])


def _sum_owner(chip_arr, pairs, got, name):
    nb, h, cols = got.shape
    tr = _row_tile(h, 16, 256)

    def body(chip_ref, own_ref, a_ref, b_ref, c_ref, o_ref):
        o_ref[...] = ((own_ref[0].astype(F32) + a_ref[0].astype(F32)) + b_ref[0].astype(F32)) + c_ref[0].astype(F32)

    def slot(off):
        return pl.BlockSpec((1, tr, cols), lambda i, chip_ref: ((chip_ref[0] + off) % N_CHIP, i, 0))

    return pl.pallas_call(
        body, name=name,
        grid_spec=pltpu.PrefetchScalarGridSpec(
            num_scalar_prefetch=1, grid=(h // tr,),
            in_specs=[slot(0), slot(1), slot(2), slot(3)],
            out_specs=pl.BlockSpec((tr, cols), lambda i, chip_ref: (i, 0))),
        out_shape=jax.ShapeDtypeStruct((h, cols), F32),
        compiler_params=_cparams(dimension_semantics=("parallel",)),
    )(chip_arr, pairs, got, got, got)


def _swap_start(arrs, name, after=()):
    n = len(arrs)
    first = 2 * n + len(after)

    def body(*refs):
        ins, lnd = refs[:n], refs[n:2 * n]
        send_sems, recv_sems = refs[first:first + n], refs[first + n:first + 2 * n]
        x, y, c, _ = _place()
        for k in range(n):
            pltpu.make_async_remote_copy(
                src_ref=ins[k], dst_ref=lnd[k], send_sem=send_sems[k], recv_sem=recv_sems[k],
                device_id=(x, y, 1 - c), device_id_type=MESH).start()

    lands = [_hbm(lax.empty(a.shape, a.dtype)) for a in arrs]
    out = pl.pallas_call(
        body, name=name,
        out_shape=(*[pltpu.SemaphoreType.DMA(())] * (2 * n), *[pltpu.HBM(a.shape, a.dtype) for a in arrs],
                   *[pltpu.HBM(a.shape, a.dtype) for a in arrs]),
        in_specs=[HBM_SPEC] * (2 * n) + [ANY_SPEC] * len(after),
        out_specs=(*[SEM_SPEC] * (2 * n), *[HBM_SPEC] * (2 * n)),
        input_output_aliases={i: 2 * n + i for i in range(2 * n)},
        compiler_params=pltpu.CompilerParams(has_side_effects=EFFECT),
    )(*[_hbm(a) for a in arrs], *lands, *after)
    return list(out[:2 * n]), list(out[2 * n:3 * n]), list(out[3 * n:4 * n])


def _swap_wait(sems, arrs, lands, after, name):
    n = len(arrs)

    def body(*refs):
        ins, lnd = refs[:n], refs[n:2 * n]
        s_sems, r_sems = refs[2 * n:3 * n], refs[3 * n:4 * n]
        x, y, c, _ = _place()
        for k in range(n):
            cp = pltpu.make_async_remote_copy(
                src_ref=ins[k], dst_ref=lnd[k], send_sem=s_sems[k], recv_sem=r_sems[k],
                device_id=(x, y, 1 - c), device_id_type=MESH)
            cp.wait_send()
            cp.wait_recv()

    out = pl.pallas_call(
        body, name=name,
        out_shape=tuple(pltpu.HBM(a.shape, a.dtype) for a in list(arrs) + list(lands)),
        in_specs=[HBM_SPEC] * (2 * n) + [SEM_SPEC] * (2 * n) + [ANY_SPEC] * len(after),
        out_specs=[HBM_SPEC] * (2 * n),
        input_output_aliases={i: i for i in range(2 * n)},
        compiler_params=pltpu.CompilerParams(has_side_effects=EFFECT),
    )(*arrs, *lands, *sems, *after)
    return list(out[:n]), list(out[n:])


def _swap_with_sibling(arrs, name):
    n = len(arrs)

    def body(*refs):
        ins, outs = refs[:n], refs[n:2 * n]
        send_sems, recv_sems = refs[2 * n:]
        x, y, c, _ = _place()
        cps = []
        for k in range(n):
            cp = pltpu.make_async_remote_copy(
                src_ref=ins[k], dst_ref=outs[k], send_sem=send_sems.at[k], recv_sem=recv_sems.at[k],
                device_id=(x, y, 1 - c), device_id_type=MESH)
            cp.start()
            cps.append(cp)
        for cp in cps:
            cp.wait()

    return pl.pallas_call(
        body, name=name,
        out_shape=[jax.ShapeDtypeStruct(a.shape, a.dtype) for a in arrs],
        in_specs=[ANY_SPEC] * n, out_specs=[ANY_SPEC] * n,
        scratch_shapes=[pltpu.SemaphoreType.DMA((n,)), pltpu.SemaphoreType.DMA((n,))],
    )(*arrs)


def _row_tile(h, mult=8, cap=128):
    for t in range(cap - cap % mult, mult - 1, -mult):
        if h % t == 0:
            return t
    if mult > 8:
        return _row_tile(h, 8, cap)
    raise ValueError(h)


def _cast_bf16(a, name):
    rows, cols = a.shape
    tr = _row_tile(rows, 16, 256)

    def body(a_ref, o_ref):
        o_ref[...] = a_ref[...].astype(BF16)

    return pl.pallas_call(
        body, name=name, grid=(rows // tr,),
        out_shape=jax.ShapeDtypeStruct(a.shape, BF16),
        in_specs=[pl.BlockSpec((tr, cols), lambda i: (i, 0))],
        out_specs=pl.BlockSpec((tr, cols), lambda i: (i, 0)),
        compiler_params=_cparams(dimension_semantics=("parallel",)),
    )(a)


def _pair_sum(c_arr, full, recv, name):
    nb, rows, cols = full.shape

    def body(c_ref, f_ref, r_ref, o_ref):
        o_ref[...] = (f_ref[...] + r_ref[...]).astype(BF16)

    if _halved_by_rows(full.shape):
        h = rows // 2
        tr = _row_tile(h, 16, 256)
        steps = h // tr
        own = pl.BlockSpec((1, tr, cols), lambda b, i, c_ref: (b, c_ref[0] * steps + i, 0))
        half = pl.BlockSpec((1, tr, cols), lambda b, i, c_ref: (b, i, 0))
    else:
        steps = 1
        own = pl.BlockSpec((1, rows, cols // 2), lambda b, i, c_ref: (b, 0, c_ref[0]))
        half = pl.BlockSpec((1, rows, cols // 2), lambda b, i, c_ref: (b, 0, 0))
    return pl.pallas_call(
        body, name=name,
        grid_spec=pltpu.PrefetchScalarGridSpec(
            num_scalar_prefetch=1, grid=(nb, steps), in_specs=[own, half], out_specs=half),
        out_shape=jax.ShapeDtypeStruct(_half_shape(full.shape), BF16),
        compiler_params=_cparams(dimension_semantics=("parallel", "parallel")),
    )(c_arr, full, recv)


def _sum_chips(got, name):
    nb, h, cols = got.shape
    tr = _row_tile(h, 16, 256)

    def body(g_ref, o_ref):
        g = g_ref[...].astype(F32)
        o_ref[...] = ((g[0] + g[1]) + g[2]) + g[3]

    return pl.pallas_call(
        body, name=name, grid=(h // tr,),
        out_shape=jax.ShapeDtypeStruct((h, cols), F32),
        in_specs=[pl.BlockSpec((nb, tr, cols), lambda i: (0, i, 0))],
        out_specs=pl.BlockSpec((tr, cols), lambda i: (i, 0)),
        compiler_params=_cparams(dimension_semantics=("parallel",)),
    )(got)


def _adam_math(g, w, m, v):
    m1 = ADAM_B1 * m + (1.0 - ADAM_B1) * g
    v1 = ADAM_B2 * v + (1.0 - ADAM_B2) * (g * g)
    m_hat = m1 / (1.0 - ADAM_B1 ** ADAM_STEP)
    v_hat = v1 / (1.0 - ADAM_B2 ** ADAM_STEP)
    delta = -ADAM_LR * (m_hat / (jnp.sqrt(v_hat) + ADAM_EPS) + ADAM_WD * w)
    return delta, m1, v1


def _adamw_halves(c_arr, own, other, w, m, v, name):
    rows, cols = w.shape
    by_rows = own.shape[1] == cols

    def body(c_ref, own_ref, oth_ref, w_ref, m_ref, v_ref, g_out, d_out, m_out, v_out):
        if by_rows:
            g = jnp.where(pl.program_id(0) == c_ref[0], own_ref[...], oth_ref[...])
        else:
            own_, oth_ = own_ref[...], oth_ref[...]
            g = jnp.where(c_ref[0] == 0, jnp.concatenate([own_, oth_], axis=1), jnp.concatenate([oth_, own_], axis=1))
        d, m1, v1 = _adam_math(g, w_ref[...], m_ref[...], v_ref[...])
        g_out[...] = g
        d_out[...] = d
        m_out[...] = m1
        v_out[...] = v1

    if by_rows:
        h = rows // 2
        tr = _row_tile(h)
        steps = h // tr
        grid = (2, steps)
        half_spec = pl.BlockSpec((tr, cols), lambda p, i, c_ref: (i, 0))
        full_spec = pl.BlockSpec((tr, cols), lambda p, i, c_ref: (p * steps + i, 0))
    else:
        tr = _row_tile(rows)
        grid = (1, rows // tr)
        half_spec = pl.BlockSpec((tr, cols // 2), lambda p, i, c_ref: (i, 0))
        full_spec = pl.BlockSpec((tr, cols), lambda p, i, c_ref: (i, 0))
    return pl.pallas_call(
        body, name=name,
        grid_spec=pltpu.PrefetchScalarGridSpec(
            num_scalar_prefetch=1, grid=grid,
            in_specs=[half_spec, half_spec, full_spec, full_spec, full_spec],
            out_specs=[full_spec] * 4),
        out_shape=[jax.ShapeDtypeStruct(w.shape, F32)] * 4,
        compiler_params=_cparams(dimension_semantics=("parallel", "parallel")),
    )(c_arr, own, other, w, m, v)


def _adamw_whole(items, name):
    n = len(items)

    def body(*refs):
        ins, outs = refs[:4 * n], refs[4 * n:]
        for k in range(n):
            g, w, m, v = (r[...] for r in ins[4 * k:4 * k + 4])
            d, m1, v1 = _adam_math(g, w, m, v)
            outs[3 * k][...] = d
            outs[3 * k + 1][...] = m1
            outs[3 * k + 2][...] = v1

    flat = [a for it in items for a in it]
    shapes = [jax.ShapeDtypeStruct(it[1].shape, F32) for it in items for _ in range(3)]
    out = pl.pallas_call(
        body, name=name, out_shape=shapes,
        in_specs=[VMEM_SPEC] * (4 * n), out_specs=[VMEM_SPEC] * (3 * n),
        compiler_params=_cparams(),
    )(*flat)
    return [tuple(out[3 * k:3 * k + 3]) for k in range(n)]


def _adamw_tiled(g, w, m, v, name):
    rows, cols = w.shape
    tr = _row_tile(rows)

    def body(g_ref, w_ref, m_ref, v_ref, d_out, m_out, v_out):
        d, m1, v1 = _adam_math(g_ref[...], w_ref[...], m_ref[...], v_ref[...])
        d_out[...] = d
        m_out[...] = m1
        v_out[...] = v1

    spec = pl.BlockSpec((tr, cols), lambda i: (i, 0))
    return pl.pallas_call(
        body, name=name, grid=(rows // tr,),
        out_shape=[jax.ShapeDtypeStruct(w.shape, F32)] * 3,
        in_specs=[spec] * 4, out_specs=[spec] * 3,
        compiler_params=_cparams(dimension_semantics=("parallel",)),
    )(g, w, m, v)


def _mod_forward(cond, w_mod, b_mod_cols, name):
    def body(c_ref, w_ref, b_ref, o_ref):
        o_ref[...] = _dot(_silu(c_ref[...]), w_ref[...]) + b_ref[...]

    return pl.pallas_call(
        body, name=name, out_shape=jax.ShapeDtypeStruct((cond.shape[0], w_mod.shape[1]), F32),
        in_specs=[VMEM_SPEC] * 3, out_specs=VMEM_SPEC, compiler_params=_cparams(),
    )(cond, w_mod, b_mod_cols)


def _mod_backward(cond, w_mod, dmod_cols, name):
    def body(c_ref, w_ref, d_ref, gw_ref, gc_ref):
        s = _silu(c_ref[...])
        d = d_ref[...]
        gw_ref[...] = _dot_tn(s, d)
        gc_ref[...] = _dot_nt(d[8:16, :], w_ref[...])

    return pl.pallas_call(
        body, name=name,
        out_shape=[jax.ShapeDtypeStruct(w_mod.shape, F32), jax.ShapeDtypeStruct((8, w_mod.shape[0]), F32)],
        in_specs=[VMEM_SPEC] * 3, out_specs=[VMEM_SPEC] * 2, compiler_params=_cparams(),
    )(cond, w_mod, dmod_cols)


def _col_chunks(width, step=512):
    return [(s, min(step, width - s)) for s in range(0, width, step)]


def _w_in_row(p_off):
    if p_off < 9 * HW:
        return p_off
    return 9 * HW if p_off == OFF_LR else p_off + 2 * RANK


def _in_projection_cols(ctx0, x0, modc, modx, pre1, w_t, name):
    d = x0.shape[1]
    n_ctx, n_lat = ctx0.shape[0], x0.shape[0]
    rows = n_ctx + n_lat
    n_main = OFF_LR // HW

    def modulated(z, mod_ref, pre_ref):
        n, _ = _rms(z)
        return (n * pre_ref[...] * (1.0 + mod_ref[1:2, :]) + mod_ref[0:1, :]).astype(BF16)

    def body(ctx_ref, x_ref, modc_ref, modx_ref, pre_ref, w_ref, h_ref, p_ref):
        j = pl.program_id(0)

        @pl.when(j == 0)
        def _():
            h_ref[0:n_ctx, :] = modulated(ctx_ref[...], modc_ref, pre_ref)
            for r0 in range(0, n_lat, TM):
                h_ref[n_ctx + r0:n_ctx + r0 + TM, :] = modulated(x_ref[r0:r0 + TM, :], modx_ref, pre_ref)

        row = pl.multiple_of(jnp.where(j < 9, j * HW, j * HW + 2 * RANK), 32)
        p_ref[...] = _dot_nt(h_ref[...], w_ref[pl.ds(row, HW), :])

    fixed = lambda j: (0, 0)
    h1, p = pl.pallas_call(
        body, name=name, grid=(n_main,),
        out_shape=[jax.ShapeDtypeStruct((rows, d), BF16), jax.ShapeDtypeStruct((rows, P_WIDTH), F32)],
        in_specs=[VMEM_SPEC, VMEM_SPEC, pl.BlockSpec((8, d), fixed), pl.BlockSpec((8, d), fixed),
                  pl.BlockSpec((1, d), fixed), VMEM_SPEC],
        out_specs=[pl.BlockSpec((rows, d), fixed), pl.BlockSpec((rows, HW), lambda j: (0, j))],
        compiler_params=_cparams(dimension_semantics=("arbitrary",)),
    )(ctx0, x0, modc, modx, pre1, w_t)

    lr0 = _w_in_row(OFF_LR)

    def lr_body(h_ref, w_ref, p_in, p_ref):
        p_ref[...] = _dot_nt(h_ref[...], w_ref[...])

    p = pl.pallas_call(
        lr_body, name=name + "_lr", grid=(1,),
        out_shape=jax.ShapeDtypeStruct((rows, P_WIDTH), F32),
        in_specs=[pl.BlockSpec((rows, d), lambda j: (0, 0)), pl.BlockSpec((128, d), lambda j: (lr0 // 128, 0)),
                  ANY_SPEC],
        out_specs=pl.BlockSpec((rows, 128), lambda j: (0, OFF_LR // 128)),
        input_output_aliases={2: 0},
        compiler_params=_cparams(),
    )(h1, w_t, p)
    return h1, p


def _in_projection(ctx0, x0, modc, modx, pre1, w_t, n_ctx_tiles, name):
    d = x0.shape[1]
    rows = ctx0.shape[0] + x0.shape[0]
    width = P_WIDTH

    def body(ctx_ref, x_ref, modc_ref, modx_ref, pre_ref, w_ref, h_ref, p_ref):
        is_ctx = pl.program_id(0) < n_ctx_tiles
        n, _ = _rms(jnp.where(is_ctx, ctx_ref[...], x_ref[...]))
        shift = jnp.where(is_ctx, modc_ref[0:1, :], modx_ref[0:1, :])
        scale = jnp.where(is_ctx, modc_ref[1:2, :], modx_ref[1:2, :])
        h = (n * pre_ref[...] * (1.0 + scale) + shift).astype(BF16)
        h_ref[...] = h
        for s, w in _col_chunks(width):
            p_ref[:, s:s + w] = _dot_nt(h, w_ref[_w_in_row(s):_w_in_row(s) + w, :])

    row = lambda i: (i, 0)
    fixed = lambda i: (0, 0)
    return pl.pallas_call(
        body, name=name, grid=(rows // TM,),
        out_shape=[jax.ShapeDtypeStruct((rows, d), BF16), jax.ShapeDtypeStruct((rows, width), F32)],
        in_specs=[pl.BlockSpec((TM, d), lambda i: (jnp.minimum(i, n_ctx_tiles - 1), 0)),
                  pl.BlockSpec((TM, d), lambda i: (jnp.maximum(i - n_ctx_tiles, 0), 0)),
                  pl.BlockSpec((8, d), fixed), pl.BlockSpec((8, d), fixed), pl.BlockSpec((1, d), fixed), VMEM_SPEC],
        out_specs=[pl.BlockSpec((TM, d), row), pl.BlockSpec((TM, width), row)],
        compiler_params=_cparams(dimension_semantics=("parallel",)),
    )(ctx0, x0, modc, modx, pre1, w_t)


C_HQ, C_HI, C_HF_FW, C_HF_BW, C_HGATE, C_GQ, C_GK, C_GV, C_GGATE = range(9)
OFF_GATE_HG = 9 * HW
OFF_LR = 13 * HW
P_WIDTH = OFF_LR + 128


def _head_norm_fwd(o, w):
    outs, ns, rs = [], [], []
    for h in range(NH):
        n, r = _rms(o[:, h * HD:(h + 1) * HD])
        ns.append(n)
        rs.append(r)
        outs.append(n * w)
    return jnp.concatenate(outs, axis=1), ns, rs


def _mixer_tail(z, o_hg, o_gla, p_hgate, p_ggate, p_gate_hg, p_gate_gla, hg_on, gla_on, wbh, wbg, wout):
    on_hg, n_hg, r_hg = _head_norm_fwd(o_hg, hg_on)
    on_gla, n_gla, r_gla = _head_norm_fwd(o_gla, gla_on)
    og_hg = (on_hg * _silu(p_hgate)).astype(BF16)
    og_gla = (on_gla * _silu(p_ggate)).astype(BF16)
    b_hg = jnp.dot(og_hg, wbh, preferred_element_type=F32)
    b_gla = jnp.dot(og_gla, wbg, preferred_element_type=F32)
    s_hg = _sigmoid(p_gate_hg)
    s_gla = _sigmoid(p_gate_gla)
    merged = (s_hg * b_hg + s_gla * b_gla).astype(BF16)
    y1 = jnp.dot(merged, wout, preferred_element_type=F32)
    return dict(on_hg=on_hg, n_hg=n_hg, r_hg=r_hg, on_gla=on_gla, n_gla=n_gla, r_gla=r_gla, og_hg=og_hg,
                og_gla=og_gla, b_hg=b_hg, b_gla=b_gla, s_hg=s_hg, s_gla=s_gla, merged=merged, y1=y1)


def _mixer_ffn(x_lat, p, o_list, modx, norms, onorms, w_br_hg, w_br_gla, w_out, w_gate, w_up, w_down, target,
               n_ctx_tiles, name):
    rows, d = x_lat.shape
    dff = w_gate.shape[0]
    inv_d = 1.0 / d

    def body(x_ref, ofw_hg, obw_hg, ofw_gla, obw_gla, p_hgate, p_ggate, p_ghg_a, p_ghg_b, p_ggla_a, p_ggla_b,
             modx_ref, norm_ref, on_ref, wbh_ref, wbg_ref, wout_ref, wg_ref, wu_ref, wd_ref, t_ref,
             loss_ref, dz2_ref, y1_ref, mrg_ref, oghg_ref, oggla_ref, h2_ref, a_ref, du_ref, dv_ref, dy2_ref,
             stat_ref):
        i = pl.program_id(0)
        post1, pre2, post2 = norm_ref[1:2, :], norm_ref[2:3, :], norm_ref[3:4, :]
        gate1, shift2, scale2, gate2 = modx_ref[2:3, :], modx_ref[3:4, :], modx_ref[4:5, :], modx_ref[5:6, :]
        p_gate_hg = jnp.concatenate([p_ghg_a[...], p_ghg_b[...]], axis=1)
        p_gate_gla = jnp.concatenate([p_ggla_a[...], p_ggla_b[...]], axis=1)
        t = _mixer_tail(x_ref[...], ofw_hg[...] + obw_hg[...], ofw_gla[...] + obw_gla[...], p_hgate[...],
                        p_ggate[...], p_gate_hg, p_gate_gla, on_ref[0:1, 0:HD], on_ref[1:2, 0:HD],
                        wbh_ref[...], wbg_ref[...], wout_ref[...])
        y1_ref[...] = t["y1"]
        mrg_ref[...] = t["merged"]
        oghg_ref[...] = t["og_hg"]
        oggla_ref[...] = t["og_gla"]
        n1, _ = _rms(t["y1"])
        z2 = x_ref[...] + n1 * post1 * gate1
        n2, r2 = _rms(z2)
        nw2 = n2 * pre2
        h2 = (nw2 * (1.0 + scale2) + shift2).astype(BF16)
        h2_ref[...] = h2
        u = _dot_nt(h2, wg_ref[...])
        v = _dot_nt(h2, wu_ref[...])
        su = _silu(u)
        a = (su * v).astype(BF16)
        a_ref[...] = a
        y2 = jnp.dot(a, wd_ref[...], preferred_element_type=F32)
        n3, r3 = _rms(y2)
        z3 = z2 + n3 * post2 * gate2
        err = z3 - t_ref[...]
        part = 0.5 * inv_d * jnp.sum(err * err)
        dz3 = err * inv_d
        dgate2 = _colsum(dz3 * n3 * post2)
        tt = dz3 * gate2
        dpost2 = _colsum(tt * n3)
        dy2 = _rms_bwd(tt * post2, n3, r3).astype(BF16)
        dy2_ref[...] = dy2
        da = _dot_nt(dy2, wd_ref[...])
        du = (da * v * _dsilu(u)).astype(BF16)
        dv = (da * su).astype(BF16)
        du_ref[...] = du
        dv_ref[...] = dv
        dh2 = (jnp.dot(du, wg_ref[...], preferred_element_type=F32)
               + jnp.dot(dv, wu_ref[...], preferred_element_type=F32))
        dshift2 = _colsum(dh2)
        dscale2 = _colsum(dh2 * nw2)
        dnw2 = dh2 * (1.0 + scale2)
        dpre2 = _colsum(dnw2 * n2)
        dz2_ref[...] = dz3 + _rms_bwd(dnw2 * pre2, n2, r2)

        @pl.when(i == 0)
        def _():
            stat_ref[...] = jnp.zeros_like(stat_ref)
            loss_ref[...] = jnp.zeros_like(loss_ref)

        for r, val in enumerate((dshift2, dscale2, dgate2, dpre2, dpost2)):
            stat_ref[r:r + 1, :] += val
        loss_ref[...] += part
        stat_ref[5:6, 0:128] += part

    tm = TM_FFN
    ctx_tiles = n_ctx_tiles * (TM // tm)
    lat = lambda i: (i, 0)
    full = lambda i: (i + ctx_tiles, 0)
    fixed = lambda i: (0, 0)

    def pcol(blk):
        return pl.BlockSpec((tm, HW), lambda i: (i + ctx_tiles, blk))

    in_specs = ([pl.BlockSpec((tm, d), lat)] + [pl.BlockSpec((tm, HW), full)] * 4
                + [pcol(C_HGATE), pcol(C_GGATE), pcol(9), pcol(10), pcol(11), pcol(12)]
                + [pl.BlockSpec((8, d), fixed), pl.BlockSpec((8, d), fixed), pl.BlockSpec((8, d), fixed)]
                + [VMEM_SPEC] * 6 + [pl.BlockSpec((tm, d), lat)])
    bf = lambda w: jax.ShapeDtypeStruct((rows, w), BF16)
    out_shape = [jax.ShapeDtypeStruct((8, 128), F32), jax.ShapeDtypeStruct((rows, d), F32),
                 jax.ShapeDtypeStruct((rows, d), F32), bf(d), bf(HW), bf(HW), bf(d), bf(dff), bf(dff), bf(dff), bf(d),
                 jax.ShapeDtypeStruct((8, d), F32)]
    out_specs = [pl.BlockSpec((8, 128), fixed), pl.BlockSpec((tm, d), lat), pl.BlockSpec((tm, d), lat),
                 pl.BlockSpec((tm, d), lat), pl.BlockSpec((tm, HW), lat), pl.BlockSpec((tm, HW), lat),
                 pl.BlockSpec((tm, d), lat), pl.BlockSpec((tm, dff), lat), pl.BlockSpec((tm, dff), lat),
                 pl.BlockSpec((tm, dff), lat), pl.BlockSpec((tm, d), lat), pl.BlockSpec((8, d), fixed)]
    return pl.pallas_call(
        body, name=name, grid=(rows // tm,), out_shape=out_shape, in_specs=in_specs, out_specs=out_specs,
        compiler_params=_cparams(dimension_semantics=("arbitrary",)),
    )(x_lat, *o_list, p, p, p, p, p, p, modx, norms, onorms, w_br_hg, w_br_gla, w_out, w_gate, w_up, w_down, target)


def _mixer_tail_fwd(x_lat, p, o_list, modx, norms, onorms, w_br_hg, w_br_gla, w_out, n_ctx_tiles, name):
    rows, d = x_lat.shape

    def body(x_ref, ofw_hg, obw_hg, ofw_gla, obw_gla, p_hgate, p_ggate, p_ghg_a, p_ghg_b, p_ggla_a, p_ggla_b,
             modx_ref, norm_ref, on_ref, wbh_ref, wbg_ref, wout_ref, z2_ref, y1_ref, mrg_ref, oghg_ref, oggla_ref):
        p_gate_hg = jnp.concatenate([p_ghg_a[...], p_ghg_b[...]], axis=1)
        p_gate_gla = jnp.concatenate([p_ggla_a[...], p_ggla_b[...]], axis=1)
        t = _mixer_tail(x_ref[...], ofw_hg[...] + obw_hg[...], ofw_gla[...] + obw_gla[...], p_hgate[...],
                        p_ggate[...], p_gate_hg, p_gate_gla, on_ref[0:1, 0:HD], on_ref[1:2, 0:HD],
                        wbh_ref[...], wbg_ref[...], wout_ref[...])
        y1_ref[...] = t["y1"]
        mrg_ref[...] = t["merged"]
        oghg_ref[...] = t["og_hg"]
        oggla_ref[...] = t["og_gla"]
        n1, _ = _rms(t["y1"])
        z2_ref[...] = x_ref[...] + n1 * norm_ref[1:2, :] * modx_ref[2:3, :]

    lat = lambda i: (i, 0)
    full = lambda i: (i + n_ctx_tiles, 0)
    fixed = lambda i: (0, 0)

    def pcol(blk):
        return pl.BlockSpec((TM, HW), lambda i: (i + n_ctx_tiles, blk))

    in_specs = ([pl.BlockSpec((TM, d), lat)] + [pl.BlockSpec((TM, HW), full)] * 4
                + [pcol(C_HGATE), pcol(C_GGATE), pcol(9), pcol(10), pcol(11), pcol(12)]
                + [pl.BlockSpec((8, d), fixed)] * 3 + [VMEM_SPEC] * 3)
    bf = lambda w: jax.ShapeDtypeStruct((rows, w), BF16)
    f32 = jax.ShapeDtypeStruct((rows, d), F32)
    return pl.pallas_call(
        body, name=name, grid=(rows // TM,), out_shape=[f32, f32, bf(d), bf(HW), bf(HW)], in_specs=in_specs,
        out_specs=[pl.BlockSpec((TM, d), lat)] * 3 + [pl.BlockSpec((TM, HW), lat)] * 2,
        compiler_params=_cparams(dimension_semantics=("parallel",)),
    )(x_lat, *o_list, p, p, p, p, p, p, modx, norms, onorms, w_br_hg, w_br_gla, w_out)


def _ffn_fwd_bwd(z2, modx, norms, w_gate, w_up, w_down, target, name):
    rows, d = z2.shape
    dff = w_gate.shape[0]
    inv_d = 1.0 / d

    def body(z2_ref, modx_ref, norm_ref, wg_ref, wu_ref, wd_ref, t_ref,
             loss_ref, dz2_ref, h2_ref, a_ref, du_ref, dv_ref, dy2_ref, stat_ref):
        i = pl.program_id(0)
        pre2, post2 = norm_ref[2:3, :], norm_ref[3:4, :]
        shift2, scale2, gate2 = modx_ref[3:4, :], modx_ref[4:5, :], modx_ref[5:6, :]
        z2 = z2_ref[...]
        n2, r2 = _rms(z2)
        nw2 = n2 * pre2
        h2 = (nw2 * (1.0 + scale2) + shift2).astype(BF16)
        h2_ref[...] = h2
        u = _dot_nt(h2, wg_ref[...])
        v = _dot_nt(h2, wu_ref[...])
        su = _silu(u)
        a = (su * v).astype(BF16)
        a_ref[...] = a
        y2 = jnp.dot(a, wd_ref[...], preferred_element_type=F32)
        n3, r3 = _rms(y2)
        err = z2 + n3 * post2 * gate2 - t_ref[...]
        part = 0.5 * inv_d * jnp.sum(err * err)
        dz3 = err * inv_d
        dgate2 = _colsum(dz3 * n3 * post2)
        tt = dz3 * gate2
        dpost2 = _colsum(tt * n3)
        dy2 = _rms_bwd(tt * post2, n3, r3).astype(BF16)
        dy2_ref[...] = dy2
        da = _dot_nt(dy2, wd_ref[...])
        du = (da * v * _dsilu(u)).astype(BF16)
        dv = (da * su).astype(BF16)
        du_ref[...] = du
        dv_ref[...] = dv
        dh2 = (jnp.dot(du, wg_ref[...], preferred_element_type=F32)
               + jnp.dot(dv, wu_ref[...], preferred_element_type=F32))
        dshift2 = _colsum(dh2)
        dscale2 = _colsum(dh2 * nw2)
        dnw2 = dh2 * (1.0 + scale2)
        dpre2 = _colsum(dnw2 * n2)
        dz2_ref[...] = dz3 + _rms_bwd(dnw2 * pre2, n2, r2)

        @pl.when(i == 0)
        def _():
            stat_ref[...] = jnp.zeros_like(stat_ref)
            loss_ref[...] = jnp.zeros_like(loss_ref)

        for r, val in enumerate((dshift2, dscale2, dgate2, dpre2, dpost2)):
            stat_ref[r:r + 1, :] += val
        loss_ref[...] += part
        stat_ref[5:6, 0:128] += part

    lat = lambda i: (i, 0)
    fixed = lambda i: (0, 0)
    bf = lambda w: jax.ShapeDtypeStruct((rows, w), BF16)
    return pl.pallas_call(
        body, name=name, grid=(rows // TM,),
        out_shape=[jax.ShapeDtypeStruct((8, 128), F32), jax.ShapeDtypeStruct((rows, d), F32), bf(d), bf(dff), bf(dff),
                   bf(dff), bf(d), jax.ShapeDtypeStruct((8, d), F32)],
        in_specs=[pl.BlockSpec((TM, d), lat), pl.BlockSpec((8, d), fixed), pl.BlockSpec((8, d), fixed)]
        + [VMEM_SPEC] * 3 + [pl.BlockSpec((TM, d), lat)],
        out_specs=[pl.BlockSpec((8, 128), fixed), pl.BlockSpec((TM, d), lat), pl.BlockSpec((TM, d), lat),
                   pl.BlockSpec((TM, dff), lat), pl.BlockSpec((TM, dff), lat), pl.BlockSpec((TM, dff), lat),
                   pl.BlockSpec((TM, d), lat), pl.BlockSpec((8, d), fixed)],
        compiler_params=_cparams(dimension_semantics=("arbitrary",)),
    )(z2, modx, norms, w_gate, w_up, w_down, target)


def _mixer_tail_bwd(x_lat, p, o_list, dz2, y1, modx, norms, onorms, w_br_hg, w_br_gla, w_out, n_ctx_tiles, n_tiles,
                    name):
    rows, d = x_lat.shape
    total = n_tiles * TM

    def body(x_ref, ofw_hg, obw_hg, ofw_gla, obw_gla, p_hgate, p_ggate, p_ghg_a, p_ghg_b, p_ggla_a, p_ggla_b,
             dz2_ref, y1_ref, modx_ref, norm_ref, on_ref, wbh_ref, wbg_ref, wout_ref,
             dohg_ref, dogla_ref, dhgate_ref, dggate_ref, dghg_ref, dggla_ref, dy1_ref, dbhg_ref, dbgla_ref,
             stat_ref):
        i = pl.program_id(0)

        @pl.when(i == 0)
        def _():
            stat_ref[...] = jnp.zeros_like(stat_ref)

        @pl.when(i < n_ctx_tiles)
        def _():
            for ref in (dohg_ref, dogla_ref, dhgate_ref, dggate_ref, dghg_ref, dggla_ref):
                ref[...] = jnp.zeros_like(ref)

        @pl.when(i >= n_ctx_tiles)
        def _():
            post1, gate1 = norm_ref[1:2, :], modx_ref[2:3, :]
            hg_on, gla_on = on_ref[0:1, 0:HD], on_ref[1:2, 0:HD]
            p_gate_hg = jnp.concatenate([p_ghg_a[...], p_ghg_b[...]], axis=1)
            p_gate_gla = jnp.concatenate([p_ggla_a[...], p_ggla_b[...]], axis=1)
            ph, pg = p_hgate[...], p_ggate[...]
            t = _mixer_tail(x_ref[...], ofw_hg[...] + obw_hg[...], ofw_gla[...] + obw_gla[...], ph, pg,
                            p_gate_hg, p_gate_gla, hg_on, gla_on, wbh_ref[...], wbg_ref[...], wout_ref[...])
            dz2 = dz2_ref[...]
            n1, r1 = _rms(y1_ref[...])
            dgate1 = _colsum(dz2 * n1 * post1)
            tt = dz2 * gate1
            dpost1 = _colsum(tt * n1)
            dy1 = _rms_bwd(tt * post1, n1, r1).astype(BF16)
            dy1_ref[...] = dy1
            dmerged = _dot_nt(dy1, wout_ref[...])
            dghg_ref[...] = dmerged * t["b_hg"] * t["s_hg"] * (1.0 - t["s_hg"])
            dggla_ref[...] = dmerged * t["b_gla"] * t["s_gla"] * (1.0 - t["s_gla"])
            db_hg = (dmerged * t["s_hg"]).astype(BF16)
            db_gla = (dmerged * t["s_gla"]).astype(BF16)
            dbhg_ref[...] = db_hg
            dbgla_ref[...] = db_gla
            don_acc = []
            for (db, wb, pgate, on, ns, rs, gain, gate_ref, do_ref) in (
                    (db_hg, wbh_ref, ph, t["on_hg"], t["n_hg"], t["r_hg"], hg_on, dhgate_ref, dohg_ref),
                    (db_gla, wbg_ref, pg, t["on_gla"], t["n_gla"], t["r_gla"], gla_on, dggate_ref, dogla_ref)):
                dog = _dot_nt(db, wb[...])
                gate_ref[...] = dog * on * _dsilu(pgate)
                don = dog * _silu(pgate)
                acc = jnp.zeros((1, HD), F32)
                for h in range(NH):
                    sl = slice(h * HD, (h + 1) * HD)
                    acc = acc + _colsum(don[:, sl] * ns[h])
                    do_ref[:, sl] = _rms_bwd(don[:, sl] * gain, ns[h], rs[h])
                don_acc.append(acc)
            stat_ref[0:1, :] += dgate1
            stat_ref[1:2, :] += dpost1
            stat_ref[2:3, 0:HD] += don_acc[0]
            stat_ref[2:3, HD:2 * HD] += don_acc[1]

    lat = lambda i: (jnp.maximum(i - n_ctx_tiles, 0), 0)
    full = lambda i: (i, 0)
    fixed = lambda i: (0, 0)

    def pcol(blk):
        return pl.BlockSpec((TM, HW), lambda i: (i, blk))

    in_specs = ([pl.BlockSpec((TM, d), lat)] + [pl.BlockSpec((TM, HW), full)] * 4
                + [pcol(C_HGATE), pcol(C_GGATE), pcol(9), pcol(10), pcol(11), pcol(12)]
                + [pl.BlockSpec((TM, d), lat), pl.BlockSpec((TM, d), lat)]
                + [pl.BlockSpec((8, d), fixed)] * 3 + [VMEM_SPEC] * 3)
    f = lambda w: jax.ShapeDtypeStruct((total, w), F32)
    out_shape = [f(HW), f(HW), f(HW), f(HW), f(d), f(d), jax.ShapeDtypeStruct((rows, d), BF16),
                 jax.ShapeDtypeStruct((rows, d), BF16), jax.ShapeDtypeStruct((rows, d), BF16),
                 jax.ShapeDtypeStruct((8, d), F32)]
    out_specs = ([pl.BlockSpec((TM, HW), full)] * 4 + [pl.BlockSpec((TM, d), full)] * 2
                 + [pl.BlockSpec((TM, d), lat)] * 3 + [pl.BlockSpec((8, d), fixed)])
    return pl.pallas_call(
        body, name=name, grid=(n_tiles,), out_shape=out_shape, in_specs=in_specs, out_specs=out_specs,
        compiler_params=_cparams(dimension_semantics=("arbitrary",)),
    )(x_lat, *o_list, p, p, p, p, p, p, dz2, y1, modx, norms, onorms, w_br_hg, w_br_gla, w_out)


def _in_projection_bwd(ctx0, x0, dz2, modc, modx, pre1, w_t, pieces, n_ctx_tiles, name):
    d = x0.shape[1]
    rows = ctx0.shape[0] + x0.shape[0]
    lat_rows = dz2.shape[0]
    width = P_WIDTH
    n_pieces = len(pieces)

    def body(*refs):
        ctx_ref, x_ref, dz2_ref, modc_ref, modx_ref, pre_ref, w_ref = refs[:7]
        (dhq_f, dhq_b, dhi_f, dhi_b, dhf_f, dhf_b, dhgate, dgq_f, dgq_b, dgk_f, dgk_b, dgv_f, dgv_b, dggate,
         dghg, dggla, dlr_f, dlr_b) = refs[7:7 + n_pieces]
        dp_ref, gx_ref, stat_ref = refs[7 + n_pieces:]
        i = pl.program_id(0)
        is_ctx = i < n_ctx_tiles
        z = jnp.where(is_ctx, ctx_ref[...], x_ref[...])
        sections = [
            (0, dhq_f[...] + dhq_b[...]), (HW, dhi_f[...] + dhi_b[...]), (2 * HW, dhf_f[...]), (3 * HW, dhf_b[...]),
            (4 * HW, dhgate[...]), (5 * HW, dgq_f[...] + dgq_b[...]), (6 * HW, dgk_f[...] + dgk_b[...]),
            (7 * HW, dgv_f[...] + dgv_b[...]), (8 * HW, dggate[...]),
            (9 * HW, dghg[:, 0:HW]), (10 * HW, dghg[:, HW:2 * HW]),
            (11 * HW, dggla[:, 0:HW]), (12 * HW, dggla[:, HW:2 * HW]), (OFF_LR, dlr_f[...] + dlr_b[...])]
        dh = jnp.zeros((TM, d), F32)
        for off, val in sections:
            w = val.shape[1]
            vb = val.astype(BF16)
            dp_ref[:, off:off + w] = vb
            dh = dh + jnp.dot(vb, w_ref[_w_in_row(off):_w_in_row(off) + w, :], preferred_element_type=F32)
        n, r = _rms(z)
        pre = pre_ref[...]
        scale = jnp.where(is_ctx, modc_ref[1:2, :], modx_ref[1:2, :])
        nw = n * pre
        dshift = _colsum(dh)
        dscale = _colsum(dh * nw)
        dnw = dh * (1.0 + scale)
        dpre = _colsum(dnw * n)
        gx_ref[...] = dz2_ref[...] + _rms_bwd(dnw * pre, n, r)
        zero = jnp.zeros((1, d), F32)

        @pl.when(i == 0)
        def _():
            stat_ref[...] = jnp.zeros_like(stat_ref)

        stat_ref[0:1, :] += jnp.where(is_ctx, zero, dshift)
        stat_ref[1:2, :] += jnp.where(is_ctx, zero, dscale)
        stat_ref[2:3, :] += jnp.where(is_ctx, dshift, zero)
        stat_ref[3:4, :] += jnp.where(is_ctx, dscale, zero)
        stat_ref[4:5, :] += dpre

    full = lambda i: (i, 0)
    lat = lambda i: (jnp.maximum(i - n_ctx_tiles, 0), 0)
    fixed = lambda i: (0, 0)
    piece_specs = [pl.BlockSpec((TM, a.shape[1]), full) for a in pieces]
    in_specs = [pl.BlockSpec((TM, d), lambda i: (jnp.minimum(i, n_ctx_tiles - 1), 0)), pl.BlockSpec((TM, d), lat),
                pl.BlockSpec((TM, d), lat), pl.BlockSpec((8, d), fixed),
                pl.BlockSpec((8, d), fixed), pl.BlockSpec((1, d), fixed), VMEM_SPEC] + piece_specs
    return pl.pallas_call(
        body, name=name, grid=(rows // TM,),
        out_shape=[jax.ShapeDtypeStruct((rows, width), BF16), jax.ShapeDtypeStruct((lat_rows, d), F32),
                   jax.ShapeDtypeStruct((8, d), F32)],
        in_specs=in_specs,
        out_specs=[pl.BlockSpec((TM, width), full), pl.BlockSpec((TM, d), lat), pl.BlockSpec((8, d), fixed)],
        compiler_params=_cparams(dimension_semantics=("arbitrary",)),
    )(ctx0, x0, dz2, modc, modx, pre1, w_t, *pieces)


def _transposed_lhs_matmul(x_ref, dy_ref, o_ref, xt_ref):
    @pl.when(pl.program_id(1) == 0)
    def _():
        xt_ref[...] = x_ref[...].T

    o_ref[...] = jnp.dot(xt_ref[...], dy_ref[...], preferred_element_type=F32)


def _w_in_grad(dp, h1, n_cols, name):
    rows, d = h1.shape
    n_main = OFF_LR // HW
    lr0 = _w_in_row(OFF_LR)

    def body(x_ref, xlr_ref, h_ref, o_hbm, xt_ref, acc_ref, sem):
        i = pl.program_id(0)

        def main_copy(step):
            row = jnp.where(step < 9, step * HW, step * HW + 2 * RANK)
            return pltpu.make_async_copy(acc_ref, o_hbm.at[pl.ds(pl.multiple_of(row, 8), HW), :], sem)

        lr_copy = pltpu.make_async_copy(acc_ref.at[0:2 * RANK, :], o_hbm.at[lr0:lr0 + 2 * RANK, :], sem)

        @pl.when(i < n_main)
        def _():
            xt_ref[...] = x_ref[...].T

        @pl.when(i > 0)
        def _():
            main_copy(i - 1).wait()

        @pl.when(i < n_main)
        def _():
            acc_ref[...] = jnp.dot(xt_ref[...], h_ref[...], preferred_element_type=F32)
            main_copy(i).start()

        @pl.when(i == n_main)
        def _():
            xt_ref[0:128, :] = xlr_ref[...].T
            acc_ref[0:128, :] = jnp.dot(xt_ref[0:128, :], h_ref[...], preferred_element_type=F32)
            lr_copy.start()
            lr_copy.wait()

    return pl.pallas_call(
        body, name=name, grid=(n_main + 1,),
        out_shape=jax.ShapeDtypeStruct((n_cols, d), F32),
        in_specs=[pl.BlockSpec((rows, HW), lambda i: (0, jnp.minimum(i, n_main - 1))),
                  pl.BlockSpec((rows, 128), lambda i: (0, OFF_LR // 128)),
                  pl.BlockSpec((rows, d), lambda i: (0, 0))],
        out_specs=ANY_SPEC,
        scratch_shapes=[pltpu.VMEM((HW, rows), BF16), pltpu.VMEM((HW, d), F32), pltpu.SemaphoreType.DMA],
        compiler_params=_cparams(dimension_semantics=("arbitrary",)),
    )(dp, dp, h1)


def _weight_grad(xs, dy, name, tk=None, tn=512, k_first=0, k_tiles=None):
    rows = dy.shape[0]
    n = dy.shape[1]
    tn_ = min(tn, n)
    tk_ = xs.shape[1] if tk is None else tk
    k_tiles = xs.shape[1] // tk_ if k_tiles is None else k_tiles
    k = k_tiles * tk_

    return pl.pallas_call(
        functools.partial(_transposed_lhs_matmul), name=name, grid=(k_tiles, n // tn_),
        out_shape=jax.ShapeDtypeStruct((k, n), F32),
        in_specs=[pl.BlockSpec((rows, tk_), lambda i, j: (0, i + k_first)),
                  pl.BlockSpec((rows, tn_), lambda i, j: (0, j))],
        out_specs=pl.BlockSpec((tk_, tn_), lambda i, j: (i, j)),
        scratch_shapes=[pltpu.VMEM((tk_, rows), BF16)],
        compiler_params=_cparams(dimension_semantics=("parallel", "arbitrary")),
    )(xs, dy)


def _running_sum(x, fw):
    c = x.shape[0]
    row = lax.broadcasted_iota(jnp.int32, (c, 1), 0)
    s = 1
    while s < c:
        if fw:
            x = x + jnp.where(row >= s, pltpu.roll(x, s, axis=0), 0.0)
        else:
            x = x + jnp.where(row < c - s, pltpu.roll(x, c - s, axis=0), 0.0)
        s *= 2
    return x


def _chunk_terms(q, k, g, fw):
    c = CHUNK
    r = lax.broadcasted_iota(jnp.int32, (c, c), 0)
    s = lax.broadcasted_iota(jnp.int32, (c, c), 1)
    causal = (s <= r) if fw else (s >= r)
    causal_t = (s >= r) if fw else (s <= r)
    cum = _running_sum(g, fw)
    row = lax.broadcasted_iota(jnp.int32, (c, 1), 0)
    pos = row if fw else (c - 1 - row)
    starts = [None]
    for j in range(1, NSUB):
        rj = SUB * j - 1 if fw else c - SUB * j
        starts.append(cum[rj:rj + 1, :])
    in_blk = [(pos >= SUB * j) & (pos < SUB * (j + 1)) for j in range(NSUB)]
    e = [jnp.exp(cum)]
    for j in range(1, NSUB):
        e.append(jnp.exp(jnp.where(pos >= SUB * j, cum - starts[j], -1e30)))
    own = jnp.zeros_like(cum)
    for j in range(1, NSUB):
        own = own + jnp.where(in_blk[j], starts[j], 0.0)
    kscale = jnp.exp(own - cum)
    rend = c - 1 if fw else 0
    cend = cum[rend:rend + 1, :]
    tail = jnp.exp(cend - cum)
    qcat = jnp.concatenate([q * e[j] for j in range(NSUB)], axis=1).astype(BF16)
    kt = k * kscale
    km = jnp.concatenate([jnp.where(in_blk[j], kt, 0.0) for j in range(NSUB)], axis=1).astype(BF16)
    return dict(causal=causal, causal_t=causal_t, e=e, in_blk=in_blk, kscale=kscale, cend=cend, tail=tail,
                qcat=qcat, km=km)


def _chunk_fwd(q, k, v, g, st0, fw):
    t = _chunk_terms(q, k, g, fw)
    a = jnp.where(t["causal"], _dot_nt(t["qcat"], t["km"]), 0.0)
    o = _dot(a, v) + _dot_nt(t["qcat"][:, 0:HD], st0)
    st1 = st0 * jnp.exp(t["cend"]) + _dot_tn(v, k * t["tail"])
    return o, st1


def _chunk_bwd(q, k, v, g, st0, do, dst1, fw):
    t = _chunk_terms(q, k, g, fw)
    qcat, km, e = t["qcat"], t["km"], t["e"]
    a_t = jnp.where(t["causal_t"], _dot_nt(km, qcat), 0.0)
    ktail = k * t["tail"]
    dv = _dot(a_t, do) + _dot_nt(ktail, dst1)
    da = jnp.where(t["causal"], _dot_nt(do, v), 0.0)
    da_t = jnp.where(t["causal_t"], _dot_nt(v, do), 0.0)
    dqcat = _dot(da, km)
    dq_inter = e[0] * _dot(do, st0)
    dq = dq_inter
    for j in range(NSUB):
        dq = dq + e[j] * dqcat[:, j * HD:(j + 1) * HD]
    dkm = _dot(da_t, qcat)
    dkt = jnp.zeros_like(k)
    for j in range(NSUB):
        dkt = dkt + jnp.where(t["in_blk"][j], dkm[:, j * HD:(j + 1) * HD], 0.0)
    dk_inter = _dot(v, dst1) * t["tail"]
    dk = dkt * t["kscale"] + dk_inter
    dcum = q * dq_inter - k * dk_inter
    for j in range(NSUB):
        sl = slice(j * HD, (j + 1) * HD)
        dcum = dcum + qcat[:, sl].astype(F32) * dqcat[:, sl] - km[:, sl].astype(F32) * dkm[:, sl]
    ecend = jnp.exp(t["cend"])
    end = ecend * _colsum(st0 * dst1) + _colsum(k * dk_inter)
    dg = _running_sum(dcum, not fw) + end
    dst0 = dst1 * ecend + _dot_tn(do, q * e[0])
    return dq, dk, dv, dg, dst0


def _running_sums(xs, fws):
    c = xs[0].shape[0]
    row = lax.broadcasted_iota(jnp.int32, (c, 1), 0)
    s = 1
    while s < c:
        xs = [x + (jnp.where(row >= s, pltpu.roll(x, s, axis=0), 0.0) if fw else
                   jnp.where(row < c - s, pltpu.roll(x, c - s, axis=0), 0.0)) for x, fw in zip(xs, fws)]
        s *= 2
    return xs


def _chunks_terms(qs, ks, gs, fws):
    c = CHUNK
    n = len(qs)
    r = lax.broadcasted_iota(jnp.int32, (c, c), 0)
    s = lax.broadcasted_iota(jnp.int32, (c, c), 1)
    row = lax.broadcasted_iota(jnp.int32, (c, 1), 0)
    per_dir = {}
    for fw in set(fws):
        pos = row if fw else (c - 1 - row)
        per_dir[fw] = dict(
            causal=(s <= r) if fw else (s >= r), causal_t=(s >= r) if fw else (s <= r), pos=pos,
            in_blk=[(pos >= SUB * j) & (pos < SUB * (j + 1)) for j in range(NSUB)],
            start_row=[None] + [SUB * j - 1 if fw else c - SUB * j for j in range(1, NSUB)],
            rend=c - 1 if fw else 0)
    dirs = [per_dir[fw] for fw in fws]
    cums = _running_sums(gs, fws)
    starts = [[None] + [cum[d["start_row"][j]:d["start_row"][j] + 1, :] for j in range(1, NSUB)]
              for cum, d in zip(cums, dirs)]
    es = [[jnp.exp(cum) for cum in cums]]
    for j in range(1, NSUB):
        es.append([jnp.exp(jnp.where(d["pos"] >= SUB * j, cum - st[j], -1e30)) for cum, st, d in zip(cums, starts, dirs)])
    owns = [sum(jnp.where(d["in_blk"][j], st[j], 0.0) for j in range(1, NSUB)) for st, d in zip(starts, dirs)]
    kscales = [jnp.exp(own - cum) for own, cum in zip(owns, cums)]
    cends = [cum[d["rend"]:d["rend"] + 1, :] for cum, d in zip(cums, dirs)]
    tails = [jnp.exp(cend - cum) for cend, cum in zip(cends, cums)]
    qcats = [jnp.concatenate([q * es[j][i] for j in range(NSUB)], axis=1).astype(BF16) for i, q in enumerate(qs)]
    kts = [k * ksc for k, ksc in zip(ks, kscales)]
    kms = [jnp.concatenate([jnp.where(d["in_blk"][j], kt, 0.0) for j in range(NSUB)], axis=1).astype(BF16)
           for kt, d in zip(kts, dirs)]
    e_by_lane = [[es[j][i] for j in range(NSUB)] for i in range(n)]
    return dict(dirs=dirs, e=e_by_lane, kscale=kscales, cend=cends, tail=tails, qcat=qcats, km=kms)


def _chunks_fwd(qs, ks, vs, gs, st0s, fws):
    t = _chunks_terms(qs, ks, gs, fws)
    scores = [_dot_nt(qc, km) for qc, km in zip(t["qcat"], t["km"])]
    a = [jnp.where(d["causal"], sc, 0.0) for sc, d in zip(scores, t["dirs"])]
    inter = [_dot_nt(qc[:, 0:HD], st0) for qc, st0 in zip(t["qcat"], st0s)]
    intra = [_dot(a_, v) for a_, v in zip(a, vs)]
    os_ = [x + y for x, y in zip(intra, inter)]
    upd = [_dot_tn(v, k * tl) for v, k, tl in zip(vs, ks, t["tail"])]
    st1s = [st0 * jnp.exp(ce) + u for st0, ce, u in zip(st0s, t["cend"], upd)]
    return os_, st1s


def _chunks_bwd(qs, ks, vs, gs, st0s, dos, dst1s, fws):
    n = len(qs)
    t = _chunks_terms(qs, ks, gs, fws)
    qcat, km, e, dirs = t["qcat"], t["km"], t["e"], t["dirs"]
    a_t = [jnp.where(d["causal_t"], _dot_nt(km_, qc), 0.0) for km_, qc, d in zip(km, qcat, dirs)]
    ktail = [k * tl for k, tl in zip(ks, t["tail"])]
    dv_a = [_dot(at, do) for at, do in zip(a_t, dos)]
    dv_b = [_dot_nt(kt, ds) for kt, ds in zip(ktail, dst1s)]
    dv = [x + y for x, y in zip(dv_a, dv_b)]
    da = [jnp.where(d["causal"], _dot_nt(do, v), 0.0) for do, v, d in zip(dos, vs, dirs)]
    da_t = [jnp.where(d["causal_t"], _dot_nt(v, do), 0.0) for do, v, d in zip(dos, vs, dirs)]
    dqcat = [_dot(da_, km_) for da_, km_ in zip(da, km)]
    dq_inter = [e[i][0] * _dot(dos[i], st0s[i]) for i in range(n)]
    dkm = [_dot(dat, qc) for dat, qc in zip(da_t, qcat)]
    dk_inter = [_dot(v, ds) * tl for v, ds, tl in zip(vs, dst1s, t["tail"])]
    dq = [dq_inter[i] + sum(e[i][j] * dqcat[i][:, j * HD:(j + 1) * HD] for j in range(NSUB)) for i in range(n)]
    dkt = [sum(jnp.where(dirs[i]["in_blk"][j], dkm[i][:, j * HD:(j + 1) * HD], 0.0) for j in range(NSUB))
           for i in range(n)]
    dk = [dkt[i] * t["kscale"][i] + dk_inter[i] for i in range(n)]
    dcum = [qs[i] * dq_inter[i] - ks[i] * dk_inter[i]
            + sum(qcat[i][:, j * HD:(j + 1) * HD].astype(F32) * dqcat[i][:, j * HD:(j + 1) * HD]
                  - km[i][:, j * HD:(j + 1) * HD].astype(F32) * dkm[i][:, j * HD:(j + 1) * HD] for j in range(NSUB))
            for i in range(n)]
    ecend = [jnp.exp(ce) for ce in t["cend"]]
    end = [ecend[i] * _colsum(st0s[i] * dst1s[i]) + _colsum(ks[i] * dk_inter[i]) for i in range(n)]
    sums = _running_sums(dcum, [not fw for fw in fws])
    dg = [sm + en for sm, en in zip(sums, end)]
    upd = [_dot_tn(dos[i], qs[i] * e[i][0]) for i in range(n)]
    dst0 = [dst1s[i] * ecend[i] + upd[i] for i in range(n)]
    return dq, dk, dv, dg, dst0


def _chunk_index(step, n_ctx_chunks, n_chunks, fw):
    if fw:
        return step
    return jnp.where(step < n_ctx_chunks, n_ctx_chunks - 1 - step, n_chunks - 1 + n_ctx_chunks - step)


def _hg_inputs(hq, hf, lbv, d_idx, sl):
    lb = _sigmoid(lbv[d_idx:d_idx + 1, sl] - lbv[2 + d_idx:3 + d_idx, sl])
    sg = _sigmoid(hf)
    f = lb + (1.0 - lb) * sg
    return _silu(hq), 1.0 - f, jnp.log(f), f, sg, lb


def _scan_fwd(p, side, n_ctx_chunks, fw, branch, name):
    rows = p.shape[0]
    n_chunks = rows // CHUNK
    d_idx = 0 if fw else 1
    hg = branch == "hg"
    cols = (C_HQ, C_HI, C_HF_FW + d_idx) if hg else (C_GQ, C_GK, C_GV)

    def body(*refs):
        if hg:
            a_ref, b_ref, c_ref, lb_ref, o_ref, st_ref, state = refs
        else:
            a_ref, b_ref, c_ref, lr_ref, wgk_ref, bgk_ref, o_ref, st_ref, state = refs
            logits = _dot(lr_ref[...], wgk_ref[...]) + bgk_ref[...]
            g_all = _log_sigmoid(logits) * (1.0 / GATE_NORM)

        @pl.when(pl.program_id(0) == 0)
        def _():
            state[...] = jnp.zeros_like(state)

        for h in range(NH):
            sl = slice(h * HD, (h + 1) * HD)
            if hg:
                q, k, g, _, _, _ = _hg_inputs(a_ref[:, sl], c_ref[:, sl], lb_ref[...], d_idx, sl)
                v = b_ref[:, sl]
            else:
                q, k, v, g = a_ref[:, sl] * (HD ** -0.5), b_ref[:, sl], c_ref[:, sl], g_all[:, sl]
            st0 = state[h]
            st_ref[0, h] = st0
            o, st1 = _chunk_fwd(q, k, v, g, st0, fw)
            o_ref[:, sl] = o
            state[h] = st1

    def cmap(blk):
        return pl.BlockSpec((CHUNK, HW), lambda j: (_chunk_index(j, n_ctx_chunks, n_chunks, fw), blk))

    fixed = lambda j: (0, 0)
    in_specs = [cmap(cols[0]), cmap(cols[1]), cmap(cols[2])]
    if hg:
        in_specs += [pl.BlockSpec((4, HW), fixed)]
        args = (p, p, p, side)
    else:
        in_specs += [pl.BlockSpec((CHUNK, 128), lambda j: (_chunk_index(j, n_ctx_chunks, n_chunks, fw), OFF_LR // 128)),
                     pl.BlockSpec((128, HW), fixed), pl.BlockSpec((1, HW), fixed)]
        args = (p, p, p, p, side[0], side[1])
    return pl.pallas_call(
        body, name=name, grid=(n_chunks,),
        out_shape=[jax.ShapeDtypeStruct((rows, HW), F32), jax.ShapeDtypeStruct((n_chunks, NH, HD, HD), F32)],
        in_specs=in_specs,
        out_specs=[pl.BlockSpec((CHUNK, HW), lambda j: (_chunk_index(j, n_ctx_chunks, n_chunks, fw), 0)),
                   pl.BlockSpec((1, NH, HD, HD), lambda j: (_chunk_index(j, n_ctx_chunks, n_chunks, fw), 0, 0, 0))],
        scratch_shapes=[pltpu.VMEM((NH, HD, HD), F32)],
        compiler_params=_cparams(dimension_semantics=("arbitrary",)),
    )(*args)


def _scan_fwd_both(p, side, n_ctx_chunks, branch, name):
    rows = p.shape[0]
    n_chunks = rows // CHUNK
    hg = branch == "hg"
    n_in = 4 if hg else 6

    def body(*refs):
        ins, outs, state = refs[:2 * n_in], refs[2 * n_in:2 * n_in + 4], refs[-1]

        @pl.when(pl.program_id(0) == 0)
        def _():
            state[...] = jnp.zeros_like(state)

        lanes, where = [], []
        for di, fw in enumerate((True, False)):
            r = ins[di * n_in:(di + 1) * n_in]
            o_ref, st_ref = outs[2 * di], outs[2 * di + 1]
            if hg:
                a_ref, b_ref, c_ref, lb_ref = r
            else:
                a_ref, b_ref, c_ref, lr_ref, wgk_ref, bgk_ref = r
                logits = _dot(lr_ref[...], wgk_ref[...]) + bgk_ref[...]
                g_all = _log_sigmoid(logits) * (1.0 / GATE_NORM)
            for h in range(NH):
                sl = slice(h * HD, (h + 1) * HD)
                if hg:
                    q, k, g, _, _, _ = _hg_inputs(a_ref[:, sl], c_ref[:, sl], lb_ref[...], di, sl)
                    v = b_ref[:, sl]
                else:
                    q, k, v, g = a_ref[:, sl] * (HD ** -0.5), b_ref[:, sl], c_ref[:, sl], g_all[:, sl]
                lanes.append((q, k, v, g, state[di, h], fw))
                where.append((di, h, sl, o_ref, st_ref))
        qs, ks, vs, gs, st0s, fws = (list(col) for col in zip(*lanes))
        os_, st1s = _chunks_fwd(qs, ks, vs, gs, st0s, fws)
        for (di, h, sl, o_ref, st_ref), st0, o, st1 in zip(where, st0s, os_, st1s):
            st_ref[0, h] = st0
            o_ref[:, sl] = o
            state[di, h] = st1

    fixed = lambda j: (0, 0)
    in_specs, args, out_specs = [], [], []
    for di, fw in enumerate((True, False)):
        chunk = functools.partial(_chunk_index, n_ctx_chunks=n_ctx_chunks, n_chunks=n_chunks, fw=fw)

        def cmap(blk, width=HW, chunk=chunk):
            return pl.BlockSpec((CHUNK, width), lambda j: (chunk(j), blk))

        if hg:
            in_specs += [cmap(C_HQ), cmap(C_HI), cmap(C_HF_FW + di), pl.BlockSpec((4, HW), fixed)]
            args += [p, p, p, side]
        else:
            in_specs += [cmap(C_GQ), cmap(C_GK), cmap(C_GV), cmap(OFF_LR // 128, 128),
                         pl.BlockSpec((128, HW), fixed), pl.BlockSpec((1, HW), fixed)]
            args += [p, p, p, p, side[di][0], side[di][1]]
        out_specs += [cmap(0), pl.BlockSpec((1, NH, HD, HD), lambda j, chunk=chunk: (chunk(j), 0, 0, 0))]
    return pl.pallas_call(
        body, name=name, grid=(n_chunks,),
        out_shape=[jax.ShapeDtypeStruct((rows, HW), F32), jax.ShapeDtypeStruct((n_chunks, NH, HD, HD), F32)] * 2,
        in_specs=in_specs, out_specs=out_specs,
        scratch_shapes=[pltpu.VMEM((2, NH, HD, HD), F32)],
        compiler_params=_cparams(dimension_semantics=("arbitrary",)),
    )(*args)


def _scan_bwd_both(p, side, states, d_o, n_ctx_chunks, branch, name):
    rows = p.shape[0]
    n_chunks = rows // CHUNK
    hg = branch == "hg"
    n_in = 6 if hg else 8
    n_out = 4 if hg else 6

    def body(*refs):
        ins, outs, dstate = refs[:2 * n_in], refs[2 * n_in:2 * n_in + 2 * n_out], refs[-1]
        first = pl.program_id(0) == 0

        @pl.when(first)
        def _():
            dstate[...] = jnp.zeros_like(dstate)

        lanes, where, extra, ctx = [], [], [], []
        for di, fw in enumerate((True, False)):
            r, w = ins[di * n_in:(di + 1) * n_in], outs[di * n_out:(di + 1) * n_out]
            if hg:
                a_ref, b_ref, c_ref, lb_ref, st_ref, do_ref = r
                da_ref, db_ref, dc_ref, dlb_ref = w
                acc_refs = (dlb_ref,)
            else:
                a_ref, b_ref, c_ref, lr_ref, wgk_ref, bgk_ref, st_ref, do_ref = r
                da_ref, db_ref, dc_ref, dlr_ref, dwgk_ref, dbias_ref = w
                acc_refs = (dwgk_ref, dbias_ref)
                lr = lr_ref[...]
                logits = _dot(lr, wgk_ref[...]) + bgk_ref[...]
                g_all = _log_sigmoid(logits) * (1.0 / GATE_NORM)

            @pl.when(first)
            def _(acc_refs=acc_refs):
                for ref in acc_refs:
                    ref[...] = jnp.zeros_like(ref)

            for h in range(NH):
                sl = slice(h * HD, (h + 1) * HD)
                if hg:
                    hq, hf = a_ref[:, sl], c_ref[:, sl]
                    q, k, g, f, sg, lb = _hg_inputs(hq, hf, lb_ref[...], di, sl)
                    v = b_ref[:, sl]
                    extra.append((hq, f, sg, lb))
                else:
                    q, k, v, g = a_ref[:, sl] * (HD ** -0.5), b_ref[:, sl], c_ref[:, sl], g_all[:, sl]
                    extra.append(None)
                lanes.append((q, k, v, g, st_ref[0, h], do_ref[:, sl], dstate[di, h], fw))
                where.append((di, h, sl))
            ctx.append((w, None if hg else (lr, logits, wgk_ref)))

        qs, ks, vs, gs, st0s, dos, dst1s, fws = (list(col) for col in zip(*lanes))
        dqs, dks, dvs, dgs, dst0s = _chunks_bwd(qs, ks, vs, gs, st0s, dos, dst1s, fws)
        dg_parts = {0: [], 1: []}
        for (di, h, sl), ex, dq, dk, dv, dg, dst0 in zip(where, extra, dqs, dks, dvs, dgs, dst0s):
            dstate[di, h] = dst0
            w = ctx[di][0]
            if hg:
                hq, f, sg, lb = ex
                da_ref, db_ref, dc_ref, dlb_ref = w
                da_ref[:, sl] = dq * _dsilu(hq)
                db_ref[:, sl] = dv
                df = dg / f - dk
                dc_ref[:, sl] = df * (1.0 - lb) * sg * (1.0 - sg)
                dlb_ref[0:1, sl] += _colsum(df * (1.0 - sg))
            else:
                da_ref, db_ref, dc_ref = w[:3]
                da_ref[:, sl] = dq * (HD ** -0.5)
                db_ref[:, sl] = dk
                dc_ref[:, sl] = dv
                dg_parts[di].append(dg)
        if not hg:
            for di in range(2):
                dlr_ref, dwgk_ref, dbias_ref = ctx[di][0][3:]
                lr, logits, wgk_ref = ctx[di][1]
                dlogits = jnp.concatenate(dg_parts[di], axis=1) * (1.0 / GATE_NORM) * (1.0 - _sigmoid(logits))
                dlr_ref[...] = _dot_nt(dlogits, wgk_ref[...])
                dwgk_ref[...] += _dot_tn(lr, dlogits)
                dbias_ref[0:1, :] += _colsum(dlogits)

    fixed = lambda j: (0, 0)
    big = jax.ShapeDtypeStruct((rows, HW), F32)
    in_specs, args, out_shape, out_specs = [], [], [], []
    for di, fw in enumerate((True, False)):
        def chunk_of(j, fw=fw):
            return _chunk_index(n_chunks - 1 - j, n_ctx_chunks, n_chunks, fw)

        def cmap(blk, width=HW, chunk_of=chunk_of):
            return pl.BlockSpec((CHUNK, width), lambda j: (chunk_of(j), blk))

        st_spec = pl.BlockSpec((1, NH, HD, HD), lambda j, chunk_of=chunk_of: (chunk_of(j), 0, 0, 0))
        if hg:
            in_specs += [cmap(C_HQ), cmap(C_HI), cmap(C_HF_FW + di), pl.BlockSpec((4, HW), fixed), st_spec, cmap(0)]
            args += [p, p, p, side, states[di], d_o]
            out_shape += [big, big, big, jax.ShapeDtypeStruct((8, HW), F32)]
            out_specs += [cmap(0), cmap(0), cmap(0), pl.BlockSpec((8, HW), fixed)]
        else:
            in_specs += [cmap(C_GQ), cmap(C_GK), cmap(C_GV), cmap(OFF_LR // 128, 128),
                         pl.BlockSpec((128, HW), fixed), pl.BlockSpec((1, HW), fixed), st_spec, cmap(0)]
            args += [p, p, p, p, side[di][0], side[di][1], states[di], d_o]
            out_shape += [big, big, big, jax.ShapeDtypeStruct((rows, 128), F32),
                          jax.ShapeDtypeStruct((128, HW), F32), jax.ShapeDtypeStruct((8, HW), F32)]
            out_specs += [cmap(0), cmap(0), cmap(0), cmap(0, 128), pl.BlockSpec((128, HW), fixed),
                          pl.BlockSpec((8, HW), fixed)]
    return pl.pallas_call(
        body, name=name, grid=(n_chunks,), out_shape=out_shape, in_specs=in_specs, out_specs=out_specs,
        scratch_shapes=[pltpu.VMEM((2, NH, HD, HD), F32)],
        compiler_params=_cparams(dimension_semantics=("arbitrary",)),
    )(*args)


def _scan_bwd(p, side, states, d_o, n_ctx_chunks, fw, branch, name):
    rows = p.shape[0]
    n_chunks = rows // CHUNK
    d_idx = 0 if fw else 1
    hg = branch == "hg"
    cols = (C_HQ, C_HI, C_HF_FW + d_idx) if hg else (C_GQ, C_GK, C_GV)

    def body(*refs):
        if hg:
            a_ref, b_ref, c_ref, lb_ref, st_ref, do_ref, da_ref, db_ref, dc_ref, dlb_ref, dstate = refs
        else:
            (a_ref, b_ref, c_ref, lr_ref, wgk_ref, bgk_ref, st_ref, do_ref, da_ref, db_ref, dc_ref, dlr_ref,
             dwgk_ref, dbias_ref, dstate) = refs
            lr = lr_ref[...]
            logits = _dot(lr, wgk_ref[...]) + bgk_ref[...]
            g_all = _log_sigmoid(logits) * (1.0 / GATE_NORM)

        @pl.when(pl.program_id(0) == 0)
        def _():
            dstate[...] = jnp.zeros_like(dstate)
            if hg:
                dlb_ref[...] = jnp.zeros_like(dlb_ref)
            else:
                dwgk_ref[...] = jnp.zeros_like(dwgk_ref)
                dbias_ref[...] = jnp.zeros_like(dbias_ref)

        dg_parts = []
        for h in range(NH):
            sl = slice(h * HD, (h + 1) * HD)
            if hg:
                hq, hf = a_ref[:, sl], c_ref[:, sl]
                q, k, g, f, sg, lb = _hg_inputs(hq, hf, lb_ref[...], d_idx, sl)
                v = b_ref[:, sl]
            else:
                q, k, v, g = a_ref[:, sl] * (HD ** -0.5), b_ref[:, sl], c_ref[:, sl], g_all[:, sl]
            dq, dk, dv, dg, dst0 = _chunk_bwd(q, k, v, g, st_ref[0, h], do_ref[:, sl], dstate[h], fw)
            dstate[h] = dst0
            if hg:
                da_ref[:, sl] = dq * _dsilu(hq)
                db_ref[:, sl] = dv
                df = dg / f - dk
                dc_ref[:, sl] = df * (1.0 - lb) * sg * (1.0 - sg)
                dlb_ref[0:1, sl] += _colsum(df * (1.0 - sg))
            else:
                da_ref[:, sl] = dq * (HD ** -0.5)
                db_ref[:, sl] = dk
                dc_ref[:, sl] = dv
                dg_parts.append(dg)
        if not hg:
            dlogits = jnp.concatenate(dg_parts, axis=1) * (1.0 / GATE_NORM) * (1.0 - _sigmoid(logits))
            dlr_ref[...] = _dot_nt(dlogits, wgk_ref[...])
            dwgk_ref[...] += _dot_tn(lr, dlogits)
            dbias_ref[0:1, :] += _colsum(dlogits)

    def chunk_of(j):
        return _chunk_index(n_chunks - 1 - j, n_ctx_chunks, n_chunks, fw)

    def cmap(blk, width=HW):
        return pl.BlockSpec((CHUNK, width), lambda j: (chunk_of(j), blk))

    fixed = lambda j: (0, 0)
    st_spec = pl.BlockSpec((1, NH, HD, HD), lambda j: (chunk_of(j), 0, 0, 0))
    big = jax.ShapeDtypeStruct((rows, HW), F32)
    if hg:
        in_specs = [cmap(cols[0]), cmap(cols[1]), cmap(cols[2]), pl.BlockSpec((4, HW), fixed), st_spec, cmap(0)]
        args = (p, p, p, side, states, d_o)
        out_shape = [big, big, big, jax.ShapeDtypeStruct((8, HW), F32)]
        out_specs = [cmap(0), cmap(0), cmap(0), pl.BlockSpec((8, HW), fixed)]
    else:
        in_specs = [cmap(cols[0]), cmap(cols[1]), cmap(cols[2]), cmap(OFF_LR // 128, 128),
                    pl.BlockSpec((128, HW), fixed), pl.BlockSpec((1, HW), fixed), st_spec, cmap(0)]
        args = (p, p, p, p, side[0], side[1], states, d_o)
        out_shape = [big, big, big, jax.ShapeDtypeStruct((rows, 128), F32), jax.ShapeDtypeStruct((128, HW), F32),
                     jax.ShapeDtypeStruct((8, HW), F32)]
        out_specs = [cmap(0), cmap(0), cmap(0), cmap(0, 128), pl.BlockSpec((128, HW), fixed),
                     pl.BlockSpec((8, HW), fixed)]
    return pl.pallas_call(
        body, name=name, grid=(n_chunks,), out_shape=out_shape, in_specs=in_specs, out_specs=out_specs,
        scratch_shapes=[pltpu.VMEM((NH, HD, HD), F32)],
        compiler_params=_cparams(dimension_semantics=("arbitrary",)),
    )(*args)


SMALL_ROWS = 56
ROWS_MOD_X = (0, 1, 8, 16, 17, 18)
ROWS_MOD_C = (2, 3)
ROW_PRE1, ROW_POST1, ROW_ONORM, ROW_PRE2, ROW_POST2, ROW_LB, ROW_BGK, ROW_WGK = 4, 9, 10, 19, 20, 24, 32, 40
ROW_LOSS = 21


def _reduce_small(gathered, lb_full, name):
    _, _, d = gathered.shape

    def body(g_ref, lb_ref, sum_ref, dmod_ref, dbmod_ref, dlb_ref):
        total = g_ref[0]
        for b in range(1, N_DEV):
            total = total + g_ref[b]
        sum_ref[...] = total
        dmod_ref[...] = jnp.zeros_like(dmod_ref)
        for m in range(N_MOD):
            col = slice(m * d, (m + 1) * d)
            acc = jnp.zeros((1, d), F32)
            for b in range(N_DEV):
                row = g_ref[b, ROWS_MOD_X[m]:ROWS_MOD_X[m] + 1, :]
                dmod_ref[b:b + 1, col] = row
                acc = acc + row
            if m < 2:
                ctx_row = total[ROWS_MOD_C[m]:ROWS_MOD_C[m] + 1, :]
                dmod_ref[8:9, col] = ctx_row
                acc = acc + ctx_row
            dbmod_ref[:, col] = acc
        lbv = lb_ref[...]
        for dd in range(2):
            lb = _sigmoid(lbv[dd:dd + 1, :] - lbv[2 + dd:3 + dd, :])
            gl = total[ROW_LB:ROW_LB + 1, dd * HW:(dd + 1) * HW] * lb * (1.0 - lb)
            dlb_ref[dd:dd + 1, :] = gl
            dlb_ref[2 + dd:3 + dd, :] = -gl

    return pl.pallas_call(
        body, name=name,
        out_shape=[jax.ShapeDtypeStruct((SMALL_ROWS, d), F32), jax.ShapeDtypeStruct((16, N_MOD * d), F32),
                   jax.ShapeDtypeStruct((1, N_MOD * d), F32), jax.ShapeDtypeStruct((4, HW), F32)],
        in_specs=[VMEM_SPEC] * 2, out_specs=[VMEM_SPEC] * 4, compiler_params=_cparams(),
    )(gathered, lb_full)


def _c_ctx_grad(gathered, c_ctx_row, name):
    def body(g_ref, c_ref, o_ref):
        acc = g_ref[0, 0:1, :]
        for chip in range(1, N_CHIP):
            acc = acc + g_ref[2 * chip, 0:1, :]
        o_ref[...] = acc * _dsilu(c_ref[...])

    return pl.pallas_call(
        body, name=name, out_shape=jax.ShapeDtypeStruct(c_ctx_row.shape, F32),
        in_specs=[VMEM_SPEC] * 2, out_specs=VMEM_SPEC, compiler_params=_cparams(),
    )(gathered, c_ctx_row)


def _relayout_w_in(w):
    pad = jnp.zeros((w.shape[0], 128 - 2 * RANK), w.dtype)
    return jnp.concatenate([w[:, :9 * HW], w[:, 9 * HW + 2 * RANK:], w[:, 9 * HW:9 * HW + 2 * RANK], pad], axis=1)


def _relayout_w_in_rows(wt):
    pad = jnp.zeros((128 - 2 * RANK, wt.shape[1]), wt.dtype)
    return jnp.concatenate([wt[:9 * HW], wt[9 * HW + 2 * RANK:], wt[9 * HW:9 * HW + 2 * RANK], pad], axis=0)


def _w_in_grad_rows(g_main, g_lr):
    return jnp.concatenate([g_main[:9 * HW], g_lr[:2 * RANK], g_main[9 * HW:]], axis=0)


def _w_in_grad_blocks(g_main, g_lr, n_blocks):
    lr0 = 9 * HW
    n = (g_main.shape[1] + 2 * RANK) // n_blocks

    def cols(lo, hi):
        out = []
        if lo < lr0:
            out.append(g_main[:, lo:min(hi, lr0)])
        if hi > lr0 and lo < lr0 + 2 * RANK:
            out.append(g_lr[:, max(lo, lr0) - lr0:min(hi, lr0 + 2 * RANK) - lr0])
        if hi > lr0 + 2 * RANK:
            out.append(g_main[:, max(lo, lr0 + 2 * RANK) - 2 * RANK:hi - 2 * RANK])
        return out

    return jnp.stack([jnp.concatenate(cols(j * n, (j + 1) * n), axis=1) for j in range(n_blocks)])


def _blocked(full, n_blocks):
    k, n = full.shape
    return full.reshape(k, n_blocks, n // n_blocks).transpose(1, 0, 2)


def _unblocked(blocks):
    nb, k, n = blocks.shape
    return blocks.transpose(1, 0, 2).reshape(k, nb * n)


def _sample_front(x0, ctx0, modc, modx, norm_pre1, lb_full, gla_side, w_in_r):
    ctx_len = ctx0.shape[0]
    n_ctx_tiles = ctx_len // TM
    n_ctx_chunks = ctx_len // CHUNK
    h1, p = _in_projection_cols(ctx0, x0, modc, modx, norm_pre1, w_in_r, "in_projection")
    o_hg_fw, st_hg_fw, o_hg_bw, st_hg_bw = _scan_fwd_both(p, lb_full, n_ctx_chunks, "hg", "scan_hg")
    o_gla_fw, st_gla_fw, o_gla_bw, st_gla_bw = _scan_fwd_both(p, gla_side, n_ctx_chunks, "gla", "scan_gla")
    return dict(h1=h1, p=p, o_list=[o_hg_fw, o_hg_bw, o_gla_fw, o_gla_bw],
                states=[st_hg_fw, st_hg_bw, st_gla_fw, st_gla_bw])


def _sample_back(reduce, front, x0, ctx0, target0, modc, modx, norm_pre1, norms, onorms, lb_full, gla_side, w_in_r,
                 wbh, wbg, wout, ffn_weights):
    seq, d = x0.shape
    ctx_len = ctx0.shape[0]
    n_ctx_tiles = ctx_len // TM
    n_tiles = (ctx_len + seq) // TM
    n_ctx_chunks = ctx_len // CHUNK
    h1, p, o_list = front["h1"], front["p"], front["o_list"]
    st_hg_fw, st_hg_bw, st_gla_fw, st_gla_bw = front["states"]
    z2, y1, merged, og_hg, og_gla = _mixer_tail_fwd(x0, p, o_list, modx, norms, onorms, wbh, wbg, wout, n_ctx_tiles,
                                                    "mixer_tail")
    wg, wu, wd = ffn_weights([z2])
    loss_part, dz2, h2, a_act, du, dv, dy2, stat_ffn = _ffn_fwd_bwd(z2, modx, norms, wg, wu, wd, target0, "ffn")
    dff = wg.shape[0]
    tok = reduce("ffn", [_weight_grad(du, h2, "grad_w_ff_gate", tk=dff // 2, tn=d),
                         _weight_grad(dv, h2, "grad_w_ff_up", tk=dff // 2, tn=d),
                         _weight_grad(a_act, dy2, "grad_w_ff_down", tk=dff // 2)])

    (d_ohg, d_ogla, d_hgate, d_ggate, d_ghg, d_ggla, dy1, db_hg, db_gla, stat_mix) = _mixer_tail_bwd(
        x0, p, o_list, dz2, y1, modx + tok, norms, onorms, wbh, wbg, wout, n_ctx_tiles, n_tiles, "mixer_tail_bwd")
    tok = reduce("mix", [_weight_grad(og_hg, db_hg, "grad_w_br_hg"), _weight_grad(og_gla, db_gla, "grad_w_br_gla"),
                         _weight_grad(merged, dy1, "grad_w_out")])
    tok = tok + reduce("push_ffn", [dy1])
    gla_b = [(wgk, bias + tok) for wgk, bias in gla_side]
    (dgq_f, dgk_f, dgv_f, dlr_f, dwgk_f, dbgk_f, dgq_b, dgk_b, dgv_b, dlr_b, dwgk_b, dbgk_b) = _scan_bwd_both(
        p, gla_b, (st_gla_fw, st_gla_bw), d_ogla, n_ctx_chunks, "gla", "scan_gla_bwd")
    lb_b = lb_full + reduce("push_mix", [dbgk_f])
    (dhq_f, dhi_f, dhf_f, dlb_f, dhq_b, dhi_b, dhf_b, dlb_b) = _scan_bwd_both(
        p, lb_b, (st_hg_fw, st_hg_bw), d_ohg, n_ctx_chunks, "hg", "scan_hg_bwd")
    pieces = [dhq_f, dhq_b, dhi_f, dhi_b, dhf_f, dhf_b, d_hgate, dgq_f, dgq_b, dgk_f, dgk_b, dgv_f, dgv_b, d_ggate,
              d_ghg, d_ggla, dlr_f, dlr_b]
    dp, grad_x, stat_in = _in_projection_bwd(ctx0, x0, dz2, modc, modx, norm_pre1, w_in_r, pieces, n_ctx_tiles,
                                             "in_projection_bwd")

    tok = reduce("in", [_w_in_grad(dp, h1, w_in_r.shape[0], "grad_w_in")])
    reduce("small", dict(stat_in=stat_in + tok, stat_mix=stat_mix, stat_ffn=stat_ffn, dlb=(dlb_f, dlb_b),
                         dwgk=(dwgk_f, dwgk_b), dbgk=(dbgk_f, dbgk_b)))
    reduce("push_in", [])
    return dict(
        loss_part=loss_part, grad_x=grad_x, stat_in=stat_in, stat_mix=stat_mix, stat_ffn=stat_ffn,
        dlb=(dlb_f, dlb_b), dwgk=(dwgk_f, dwgk_b), dbgk=(dbgk_f, dbgk_b))


def kernel(x, c, ctx, c_ctx, w_mod, b_mod, norm_pre1, norm_post1, norm_pre2, norm_post2, w_in, hg_lb, hg_onorm, gla_w_gk, gla_b_gk, gla_onorm, w_br_hg, w_br_gla, w_out, w_ff_gate, w_ff_up, w_ff_down, loss_target, m_c_ctx, m_w_mod, m_b_mod, m_norm_pre1, m_norm_post1, m_norm_pre2, m_norm_post2, m_w_in, m_hg_lb, m_hg_onorm, m_gla_w_gk, m_gla_b_gk, m_gla_onorm, m_w_br_hg, m_w_br_gla, m_w_out, m_w_ff_gate, m_w_ff_up, m_w_ff_down, v_c_ctx, v_w_mod, v_b_mod, v_norm_pre1, v_norm_post1, v_norm_pre2, v_norm_post2, v_w_in, v_hg_lb, v_hg_onorm, v_gla_w_gk, v_gla_b_gk, v_gla_onorm, v_w_br_hg, v_w_br_gla, v_w_out, v_w_ff_gate, v_w_ff_up, v_w_ff_down):
    seq, d = x.shape[1], x.shape[2]
    ctx_len = ctx.shape[1]
    assert seq % TM == 0 and ctx_len % TM == 0 and d == 2 * HW
    ax, ay, ac = lax.axis_index("x"), lax.axis_index("y"), lax.axis_index("c")
    chip = 2 * ax + ay
    dev = 2 * chip + ac
    c_arr = jnp.reshape(ac, (1,)).astype(jnp.int32)
    chip_arr = jnp.reshape(chip, (1,)).astype(jnp.int32)
    transposed = ("w_in", "w_ff_gate", "w_ff_up")
    view = lambda a, nm: a[0].T if nm in transposed else a[0]

    sems_in, lands_in, token_in0 = _blocks_start([_cast_into_blocks(chip_arr, view(w_in, "w_in"), "cast_w_in")],
                                                 "gather_w_in_start")

    nc = d // 128
    pad8 = lambda a: jnp.pad(a, ((0, -a.shape[0] % 8), (0, 0)))
    small1 = jnp.concatenate([c.reshape(nc, 128) + token_in0[0, 0], pad8(hg_lb.reshape(4, 128)),
                              gla_w_gk.reshape(2 * RANK, 128), pad8(gla_b_gk.reshape(2, 128))], axis=0)
    blocks = [_cast_into_blocks(chip_arr, view(w_, nm), "cast_" + nm) for w_, nm in (
        (w_br_hg, "w_br_hg"), (w_br_gla, "w_br_gla"), (w_out, "w_out"), (w_ff_gate, "w_ff_gate"),
        (w_ff_up, "w_ff_up"), (w_ff_down, "w_ff_down"))]
    got1 = _allgather8(small1, "gather_small_params", after=blocks)
    c_all = got1[:, :nc, :].reshape(N_DEV, d)
    per_chip = got1[0::2]
    lb_full = per_chip[:, nc:nc + 4, :].transpose(1, 0, 2).reshape(4, HW)
    wgk_full = per_chip[:, nc + 8:nc + 8 + 2 * RANK, :].transpose(1, 0, 2).reshape(2, RANK, HW)
    bgk_full = per_chip[:, nc + 8 + 2 * RANK:nc + 10 + 2 * RANK, :].transpose(1, 0, 2).reshape(2, HW)
    wgk_pad = [jnp.zeros((128, HW), F32).at[dd * RANK:(dd + 1) * RANK].set(wgk_full[dd]) for dd in range(2)]
    bgk = [bgk_full[dd:dd + 1] for dd in range(2)]

    n_mod_cols = w_mod.shape[2]
    cond = jnp.concatenate([c_all, pad8(c_ctx.reshape(1, d))], axis=0)
    b_cols = lax.dynamic_slice(b_mod, (0, chip * n_mod_cols), (1, n_mod_cols))
    mod_part = _mod_forward(cond, w_mod[0], b_cols, "mod_forward")
    mod_got = _allgather8(mod_part, "gather_mod")
    mod_all = mod_got[0::2].transpose(1, 0, 2).reshape(16, N_CHIP * n_mod_cols)
    modx = pad8(lax.dynamic_slice(mod_all, (dev, 0), (1, N_MOD * d)).reshape(N_MOD, d))
    modc = pad8(mod_all[8].reshape(N_MOD, d))

    lands_in = _blocks_wait(sems_in, lands_in, [mod_got], "gather_w_in_wait")
    gathered_in = _blocks_finish(lands_in, "gather_w_in_finish")
    sems, lands, token = _blocks_start(blocks, "gather_rest_start", after=[gathered_in[0]])
    w_in_r = gathered_in[0].reshape(-1, d)

    norms = jnp.concatenate([norm_pre1, norm_post1, norm_pre2, norm_post2, jnp.zeros((4, d), F32)], axis=0)
    onorms = jnp.zeros((8, d), F32).at[0, :HD].set(hg_onorm[0]).at[1, :HD].set(gla_onorm[0])
    gla_side = [(wgk_pad[dd], bgk[dd]) for dd in range(2)]
    modx = modx + token[0, 0]
    front = _sample_front(x[0], ctx[0], modc, modx, norm_pre1, lb_full, gla_side, w_in_r)
    lands = _blocks_wait(sems, lands, front["o_list"], "gather_rest_wait")
    gathered = _blocks_finish(lands[:3], "gather_mix_finish")
    wbh, wbg = _unblocked(gathered[0]), _unblocked(gathered[1])
    wout = gathered[2].reshape(d, d)
    ffn_sems, ffn_lands, ffn_token = _forward_start(lands[3:], "gather_ffn_forward_start")
    onorms = onorms + ffn_token[0, 0]

    def ffn_weights(after):
        got = _forward_wait(ffn_sems, ffn_lands, after, "gather_ffn_forward_wait")
        return tuple(g.reshape(-1, d) for g in got)

    dff = w_ff_down.shape[1] * N_CHIP
    groups = {"ffn": ["w_ff_gate", "w_ff_up", "w_ff_down"], "mix": ["w_br_hg", "w_br_gla", "w_out"], "in": ["w_in"]}
    row_sharded = {"w_out": d // N_CHIP, "w_ff_down": dff // N_CHIP, "w_ff_gate": dff // N_CHIP,
                   "w_ff_up": dff // N_CHIP, "w_in": w_in.shape[2]}
    in_flight, to_sibling = {}, {}

    small = {}

    def reduce_small(stats):
        small2 = jnp.concatenate([
            stats["stat_in"], stats["stat_mix"], stats["stat_ffn"],
            jnp.concatenate(stats["dlb"], axis=1), jnp.concatenate(stats["dbgk"], axis=1),
            jnp.concatenate([stats["dwgk"][0][0:RANK], stats["dwgk"][1][RANK:2 * RANK]], axis=1)], axis=0)
        assert small2.shape[0] == SMALL_ROWS
        got2 = _allgather8(small2, "gather_small_grads")
        total, dmod_all, g_b_mod, g_lb_full = _reduce_small(got2, lb_full, "reduce_small")
        dmod_cols = lax.dynamic_slice(dmod_all, (0, chip * n_mod_cols), (16, n_mod_cols))
        g_w_mod, cctx_part = _mod_backward(cond, w_mod[0], dmod_cols, "mod_backward")
        got3 = _allgather8(cctx_part, "gather_c_ctx_grad")
        g_c_ctx = _c_ctx_grad(got3, c_ctx.reshape(1, d), "c_ctx_grad")
        small.update(total=total, g_b_mod=g_b_mod, g_lb_full=g_lb_full, g_w_mod=g_w_mod, g_c_ctx=g_c_ctx)

    def reduce(group, grads):
        if group == "small":
            return reduce_small(grads)
        if group.startswith("push_"):
            return push(group[5:], grads)
        nms = groups[group]
        full = [g.reshape(N_CHIP, row_sharded[nm], d) if nm in row_sharded else _blocked(g, N_CHIP)
                for g, nm in zip(grads, nms)]
        sems_, full, lands_, token_ = _send_half_start(full, "grads_to_sibling_start_" + group)
        to_sibling[group] = (sems_, full, lands_)
        return token_[0, 0]

    def push(group, after):
        nms = groups[group]
        sems_, full, lands_ = to_sibling[group]
        if group == "in":
            after = list(after) + [small["g_c_ctx"], small["total"]]
        full, from_sibling = _send_half_wait(sems_, full, lands_, after, "grads_to_sibling_wait_" + group)
        pairs = [_pair_sum(c_arr, f, r_, "pair_sum_" + nm) for f, r_, nm in zip(full, from_sibling, nms)]
        after = [small["g_c_ctx"], small["total"]] if group == "in" else []
        sems_, pairs, lands_, token_ = _scatter_start(pairs, "grads_to_owner_start_" + group, after)
        in_flight[group] = (sems_, pairs, lands_, token_)
        return token_[0, 0]

    r = _sample_back(reduce, front, x[0], ctx[0], loss_target[0], modc, modx, norm_pre1, norms, onorms, lb_full,
                     gla_side, w_in_r, wbh, wbg, wout, ffn_weights)
    loss_part, grad_x, stat_in, stat_mix, stat_ffn = (r[k] for k in ("loss_part", "grad_x", "stat_in", "stat_mix",
                                                                     "stat_ffn"))
    (dlb_f, dlb_b), (dwgk_f, dwgk_b), (dbgk_f, dbgk_b) = r["dlb"], r["dwgk"], r["dbgk"]

    weights = dict(w_in=(w_in, m_w_in, v_w_in), w_br_hg=(w_br_hg, m_w_br_hg, v_w_br_hg),
                   w_br_gla=(w_br_gla, m_w_br_gla, v_w_br_gla), w_out=(w_out, m_w_out, v_w_out),
                   w_ff_gate=(w_ff_gate, m_w_ff_gate, v_w_ff_gate), w_ff_up=(w_ff_up, m_w_ff_up, v_w_ff_up),
                   w_ff_down=(w_ff_down, m_w_ff_down, v_w_ff_down))
    names = ["w_in", "w_br_hg", "w_br_gla", "w_out", "w_ff_gate", "w_ff_up", "w_ff_down"]
    big = {}

    swapping = {}

    def sum_and_swap(group, after):
        sems_, pairs, lands_, _ = in_flight[group]
        pairs, lands_ = _scatter_wait(sems_, pairs, lands_, after, "grads_to_owner_wait_" + group)
        own_half = [_sum_owner(chip_arr, pr, g, "chip_sum_" + nm) for pr, g, nm in zip(pairs, lands_, groups[group])]
        swapping[group] = _swap_start(own_half, "halves_to_sibling_start_" + group)
        return own_half[-1]

    def update(group, after):
        sems_, own_half, lands_ = swapping[group]
        own_half, other_half = _swap_wait(sems_, own_half, lands_, after, "halves_to_sibling_wait_" + group)
        done = []
        for nm, own, oth in zip(groups[group], own_half, other_half):
            w_, m_, v_ = (view(a, nm) for a in weights[nm])
            res = _adamw_halves(c_arr, own, oth, w_, m_, v_, "adamw_" + nm)
            big[nm] = [r_.T[None] if nm in transposed else r_[None] for r_ in res]
            done.append(res[1])
        return done

    token_in = in_flight["in"][3]
    summed_ffn = sum_and_swap("ffn", [token_in])
    summed_mix = sum_and_swap("mix", [summed_ffn])

    total, g_b_mod, g_lb_full, g_w_mod, g_c_ctx = (small[k] for k in ("total", "g_b_mod", "g_lb_full", "g_w_mod",
                                                                      "g_c_ctx"))
    g_pre1, g_post1, g_pre2, g_post2 = (total[r_:r_ + 1] for r_ in (ROW_PRE1, ROW_POST1, ROW_PRE2, ROW_POST2))
    g_hg_on, g_gla_on = total[ROW_ONORM:ROW_ONORM + 1, 0:HD], total[ROW_ONORM:ROW_ONORM + 1, HD:2 * HD]
    n_lb = hg_lb.shape[2]
    g_hg_lb = lax.dynamic_slice(g_lb_full, (0, chip * n_lb), (4, n_lb))
    g_bgk = lax.dynamic_slice(total[ROW_BGK:ROW_BGK + 1].reshape(2, HW), (0, chip * n_lb), (2, n_lb))
    g_wgk_full = total[ROW_WGK:ROW_WGK + RANK].reshape(RANK, 2, HW).transpose(1, 0, 2).reshape(2 * RANK, HW)
    g_wgk = lax.dynamic_slice(g_wgk_full, (0, chip * n_lb), (2 * RANK, n_lb))

    small_items = [
        (g_c_ctx, c_ctx.reshape(1, d), m_c_ctx.reshape(1, d), v_c_ctx.reshape(1, d)),
        (g_b_mod, b_mod, m_b_mod, v_b_mod),
        (g_pre1, norm_pre1, m_norm_pre1, v_norm_pre1),
        (g_post1, norm_post1, m_norm_post1, v_norm_post1),
        (g_pre2, norm_pre2, m_norm_pre2, v_norm_pre2),
        (g_post2, norm_post2, m_norm_post2, v_norm_post2),
        (g_hg_lb, hg_lb.reshape(4, n_lb), m_hg_lb.reshape(4, n_lb), v_hg_lb.reshape(4, n_lb)),
        (g_hg_on, hg_onorm, m_hg_onorm, v_hg_onorm),
        (g_wgk, gla_w_gk.reshape(2 * RANK, n_lb), m_gla_w_gk.reshape(2 * RANK, n_lb), v_gla_w_gk.reshape(2 * RANK, n_lb)),
        (g_bgk, gla_b_gk.reshape(2, n_lb), m_gla_b_gk.reshape(2, n_lb), v_gla_b_gk.reshape(2, n_lb)),
        (g_gla_on, gla_onorm, m_gla_onorm, v_gla_onorm),
    ]
    small_res = _adamw_whole(small_items, "adamw_small")
    mod_res = _adamw_tiled(g_w_mod, w_mod[0], m_w_mod[0], v_w_mod[0], "adamw_w_mod")
    done_ffn = update("ffn", [summed_mix, mod_res[0], small_res[0][0]])
    done_mix = update("mix", done_ffn)
    update("in", [sum_and_swap("in", done_mix)])

    loss = total[ROW_LOSS, 0]

    shapes = dict(c_ctx=c_ctx.shape, b_mod=b_mod.shape, norm_pre1=norm_pre1.shape, norm_post1=norm_post1.shape,
                  norm_pre2=norm_pre2.shape, norm_post2=norm_post2.shape, hg_lb=hg_lb.shape, hg_onorm=hg_onorm.shape,
                  gla_w_gk=gla_w_gk.shape, gla_b_gk=gla_b_gk.shape, gla_onorm=gla_onorm.shape)
    small_names = ["c_ctx", "b_mod", "norm_pre1", "norm_post1", "norm_pre2", "norm_post2", "hg_lb", "hg_onorm",
                   "gla_w_gk", "gla_b_gk", "gla_onorm"]
    grads, deltas, new_m, new_v = {}, {}, {}, {}
    for nm, item, res in zip(small_names, small_items, small_res):
        grads[nm] = item[0].reshape(shapes[nm])
        deltas[nm], new_m[nm], new_v[nm] = (r.reshape(shapes[nm]) for r in res)
    grads["w_mod"] = g_w_mod[None]
    deltas["w_mod"], new_m["w_mod"], new_v["w_mod"] = (r[None] for r in mod_res)
    for nm in names:
        grads[nm], deltas[nm], new_m[nm], new_v[nm] = big[nm]
    order = ["c_ctx", "w_mod", "b_mod", "norm_pre1", "norm_post1", "norm_pre2", "norm_post2", "w_in", "hg_lb",
             "hg_onorm", "gla_w_gk", "gla_b_gk", "gla_onorm", "w_br_hg", "w_br_gla", "w_out", "w_ff_gate", "w_ff_up",
             "w_ff_down"]
    return (loss, grad_x[None], *[grads[n] for n in order], *[deltas[n] for n in order],
            *[new_m[n] for n in order], *[new_v[n] for n in order])


def _weight_grad_cols(xs, dy, n_cols, name, tn=512):
    rows = dy.shape[0]
    k = tk = xs.shape[1]

    return pl.pallas_call(
        functools.partial(_transposed_lhs_matmul), name=name, grid=(k // tk, n_cols // tn),
        out_shape=jax.ShapeDtypeStruct((k, n_cols), F32),
        in_specs=[pl.BlockSpec((rows, tk), lambda i, j: (0, i)), pl.BlockSpec((rows, tn), lambda i, j: (0, j))],
        out_specs=pl.BlockSpec((tk, tn), lambda i, j: (i, j)),
        scratch_shapes=[pltpu.VMEM((tk, rows), BF16)],
        compiler_params=_cparams(dimension_semantics=("parallel", "arbitrary")),
    )(xs, dy)
```

```python
import functools

import jax
import jax.numpy as jnp
from jax import lax
from jax.experimental import pallas as pl
from jax.experimental.pallas import tpu as pltpu

F32 = jnp.float32
BF16 = jnp.bfloat16
HIGHEST = lax.Precision.HIGHEST
MESH = pl.DeviceIdType.MESH

EPS = 1e-6
CHUNK = 64
SUB = 16
NSUB = CHUNK // SUB
NH = 4
HD = 128
HW = NH * HD
RANK = 16
GATE_NORM = 16.0
N_MOD = 6
TM = 256
TM_FFN = 128
N_DEV = 8
N_CHIP = 4
VMEM_LIMIT = 56 * 1024 * 1024

ADAM_LR = 0.001
ADAM_B1 = 0.9
ADAM_B2 = 0.999
ADAM_EPS = 1e-08
ADAM_WD = 0.01
ADAM_STEP = 10

VMEM_SPEC = pl.BlockSpec(memory_space=pltpu.VMEM)
ANY_SPEC = pl.BlockSpec(memory_space=pl.ANY)
HBM_SPEC = pl.BlockSpec(memory_space=pltpu.HBM)
SEM_SPEC = pl.BlockSpec(memory_space=pltpu.SEMAPHORE)
EFFECT = pltpu.SideEffectType.DATAFLOW_SIDE_EFFECTING


def _cparams(**kw):
    return pltpu.CompilerParams(vmem_limit_bytes=VMEM_LIMIT, **kw)


def _dot(a, b):
    return jnp.dot(a.astype(BF16), b.astype(BF16), preferred_element_type=F32)


def _dot_nt(a, b):
    return lax.dot_general(a.astype(BF16), b.astype(BF16), (((1,), (1,)), ((), ())), preferred_element_type=F32)


def _dot_tn(a, b):
    return lax.dot_general(a.astype(BF16), b.astype(BF16), (((0,), (0,)), ((), ())), preferred_element_type=F32)


def _sigmoid(x):
    return 1.0 / (1.0 + jnp.exp(-x))


def _silu(x):
    return x * _sigmoid(x)


def _dsilu(x):
    s = _sigmoid(x)
    return s * (1.0 + x * (1.0 - s))


def _log_sigmoid(x):
    return jnp.minimum(x, 0.0) - jnp.log(1.0 + jnp.exp(-jnp.abs(x)))


def _colsum(a):
    return jnp.sum(a, axis=0, keepdims=True)


def _rms(a):
    r = lax.rsqrt(jnp.mean(a * a, axis=-1, keepdims=True) + EPS)
    return a * r, r


def _rms_bwd(dn, n, r):
    return r * (dn - n * jnp.mean(dn * n, axis=-1, keepdims=True))


def _place():
    x, y, c = lax.axis_index("x"), lax.axis_index("y"), lax.axis_index("c")
    chips = [(1 - x, y), (x, 1 - y), (1 - x, 1 - y)]
    return x, y, c, chips


def _allgather8(v, name, after=()):
    rows, cols = v.shape
    n_after = len(after)

    def body(x_ref, *rest):
        out_ref, send_sems, recv_sems, local_sem = rest[n_after:]
        x, y, c, chips = _place()
        me, sibling = (x, y, c), (x, y, 1 - c)

        def blk(px, py, pc):
            return out_ref.at[4 * px + 2 * py + pc]

        def copy(k, block, to, src=None):
            return pltpu.make_async_remote_copy(
                src_ref=blk(*block) if src is None else src, dst_ref=blk(*block),
                send_sem=send_sems.at[k], recv_sem=recv_sems.at[k], device_id=to, device_id_type=MESH)

        mine = pltpu.make_async_copy(x_ref, blk(*me), local_sem)
        mine.start()
        first = [copy(0, me, sibling, src=x_ref)]
        first += [copy(1 + j, me, (*chip, c), src=x_ref) for j, chip in enumerate(chips)]
        for cp in first:
            cp.start()
        passed = [copy(4 + j, (*chip, c), sibling) for j, chip in enumerate(chips)]
        for j, chip in enumerate(chips):
            copy(1 + j, (*chip, c), me).wait_recv()
            passed[j].start()
        copy(0, sibling, me).wait_recv()
        for j, chip in enumerate(chips):
            copy(4 + j, (*chip, 1 - c), me).wait_recv()
        for cp in first + passed:
            cp.wait_send()
        mine.wait()

    return pl.pallas_call(
        body, name=name,
        out_shape=jax.ShapeDtypeStruct((N_DEV, rows, cols), v.dtype),
        in_specs=[VMEM_SPEC] + [ANY_SPEC] * n_after, out_specs=VMEM_SPEC,
        scratch_shapes=[pltpu.SemaphoreType.DMA((7,)), pltpu.SemaphoreType.DMA((7,)), pltpu.SemaphoreType.DMA],
    )(v, *after)


def _cast_into_blocks(chip_arr, w, name):
    rows, cols = w.shape
    tr = _row_tile(rows, 16, 256)

    def body(chip_ref, w_ref, o_ref):
        o_ref[0] = w_ref[...].astype(BF16)

    return pl.pallas_call(
        body, name=name,
        grid_spec=pltpu.PrefetchScalarGridSpec(
            num_scalar_prefetch=1, grid=(rows // tr,),
            in_specs=[pl.BlockSpec((tr, cols), lambda i, chip_ref: (i, 0))],
            out_specs=pl.BlockSpec((1, tr, cols), lambda i, chip_ref: (chip_ref[0], i, 0))),
        out_shape=jax.ShapeDtypeStruct((N_CHIP, rows, cols), BF16),
        compiler_params=_cparams(dimension_semantics=("parallel",)),
    )(chip_arr, w)


def _halved_by_rows(shape):
    return (shape[1] // 2) % 16 == 0


def _half_of(ref, pc, block=None):
    lead = slice(None) if block is None else block
    if _halved_by_rows(ref.shape):
        h = ref.shape[1] // 2
        return ref.at[lead, pl.ds(pl.multiple_of(pc * h, 16), h), :]
    h = ref.shape[2] // 2
    return ref.at[lead, :, pl.ds(pl.multiple_of(pc * h, 128), h)]


def _half_shape(shape):
    return (shape[0], shape[1] // 2, shape[2]) if _halved_by_rows(shape) else (shape[0], shape[1], shape[2] // 2)


def _half_rows(ref, chip_id, pc):
    return _half_of(ref, pc, chip_id)


def _gather_blocks(lands, name, after=()):
    n = len(lands)
    n_in = n + len(after)

    def body(*refs):
        outs = refs[n_in:n_in + n]
        send_sems, recv_sems = refs[n_in + n:]
        x, y, c, chips = _place()
        me_chip = 2 * x + y
        sibling = (x, y, 1 - c)

        def copy(k, j, chip_id, pc, to):
            return pltpu.make_async_remote_copy(
                src_ref=_half_rows(outs[k], chip_id, pc), dst_ref=_half_rows(outs[k], chip_id, pc),
                send_sem=send_sems.at[k, j], recv_sem=recv_sems.at[k, j], device_id=to, device_id_type=MESH)

        started = []
        for k in range(n):
            for j, chip in enumerate(chips):
                cp = copy(k, j, me_chip, c, (*chip, c))
                cp.start()
                started.append(cp)
        for k in range(n):
            for j, (px, py) in enumerate(chips):
                copy(k, j, 2 * px + py, c, sibling).wait_recv()
                cp = copy(k, 3 + j, 2 * px + py, c, sibling)
                cp.start()
                started.append(cp)
        for k in range(n):
            for j, (px, py) in enumerate(chips):
                copy(k, 3 + j, 2 * px + py, 1 - c, sibling).wait_recv()
        for cp in started:
            cp.wait_send()

    return pl.pallas_call(
        body, name=name,
        out_shape=[jax.ShapeDtypeStruct(l.shape, l.dtype) for l in lands],
        in_specs=[ANY_SPEC] * n_in, out_specs=[ANY_SPEC] * n,
        input_output_aliases={i: i for i in range(n)},
        scratch_shapes=[pltpu.SemaphoreType.DMA((n, 6)), pltpu.SemaphoreType.DMA((n, 6))],
    )(*lands, *after)


def _hbm(a):
    return pltpu.with_memory_space_constraint(a, pltpu.HBM)


def _blocks_start(lands, name, after=()):
    n = len(lands)
    n_sem = 3 * n
    first = n + len(after)

    def body(*refs):
        lnd = refs[:n]
        send_sems, recv_sems = refs[first:first + n_sem], refs[first + n_sem:first + 2 * n_sem]
        token = refs[-1]
        x, y, c, chips = _place()
        me_chip = 2 * x + y
        for k in range(n):
            for j, chip in enumerate(chips):
                pltpu.make_async_remote_copy(
                    src_ref=_half_rows(lnd[k], me_chip, c), dst_ref=_half_rows(lnd[k], me_chip, c),
                    send_sem=send_sems[3 * k + j], recv_sem=recv_sems[3 * k + j],
                    device_id=(*chip, c), device_id_type=MESH).start()
        token[...] = jnp.zeros_like(token)

    out = pl.pallas_call(
        body, name=name,
        out_shape=(*[pltpu.SemaphoreType.DMA(())] * (2 * n_sem),
                   *[pltpu.HBM(l.shape, l.dtype) for l in lands],
                   jax.ShapeDtypeStruct((8, 128), F32)),
        in_specs=[HBM_SPEC] * n + [ANY_SPEC] * len(after),
        out_specs=(*[SEM_SPEC] * (2 * n_sem), *[HBM_SPEC] * n, VMEM_SPEC),
        input_output_aliases={i: 2 * n_sem + i for i in range(n)},
        compiler_params=pltpu.CompilerParams(has_side_effects=EFFECT),
    )(*[_hbm(l) for l in lands], *after)
    return list(out[:2 * n_sem]), list(out[2 * n_sem:2 * n_sem + n]), out[-1]


def _blocks_wait(sems, lands, after, name):
    n = len(lands)
    n_sem = 3 * n

    def body(*refs):
        lnd = refs[:n]
        s_sems, r_sems = refs[n:n + n_sem], refs[n + n_sem:n + 2 * n_sem]
        x, y, c, chips = _place()
        me_chip = 2 * x + y
        for k in range(n):
            for j, (px, py) in enumerate(chips):
                cp = pltpu.make_async_remote_copy(
                    src_ref=_half_rows(lnd[k], me_chip, c), dst_ref=_half_rows(lnd[k], 2 * px + py, c),
                    send_sem=s_sems[3 * k + j], recv_sem=r_sems[3 * k + j],
                    device_id=(px, py, c), device_id_type=MESH)
                cp.wait_send()
                cp.wait_recv()

    out = pl.pallas_call(
        body, name=name,
        out_shape=tuple(pltpu.HBM(l.shape, l.dtype) for l in lands),
        in_specs=[HBM_SPEC] * n + [SEM_SPEC] * (2 * n_sem) + [ANY_SPEC] * len(after),
        out_specs=[HBM_SPEC] * n,
        input_output_aliases={i: i for i in range(n)},
        compiler_params=pltpu.CompilerParams(has_side_effects=EFFECT),
    )(*lands, *sems, *after)
    return list(out)


def _forward_start(lands, name):
    n = len(lands)
    n_sem = 3 * n

    def body(*refs):
        lnd = refs[:n]
        send_sems, recv_sems = refs[n:n + n_sem], refs[n + n_sem:n + 2 * n_sem]
        x, y, c, chips = _place()
        for k in range(n):
            for j, (px, py) in enumerate(chips):
                pltpu.make_async_remote_copy(
                    src_ref=_half_rows(lnd[k], 2 * px + py, c), dst_ref=_half_rows(lnd[k], 2 * px + py, c),
                    send_sem=send_sems[3 * k + j], recv_sem=recv_sems[3 * k + j],
                    device_id=(x, y, 1 - c), device_id_type=MESH).start()
        refs[-1][...] = jnp.zeros_like(refs[-1])

    out = pl.pallas_call(
        body, name=name,
        out_shape=(*[pltpu.SemaphoreType.DMA(())] * (2 * n_sem), *[pltpu.HBM(l.shape, l.dtype) for l in lands],
                   jax.ShapeDtypeStruct((8, 128), F32)),
        in_specs=[HBM_SPEC] * n,
        out_specs=(*[SEM_SPEC] * (2 * n_sem), *[HBM_SPEC] * n, VMEM_SPEC),
        input_output_aliases={i: 2 * n_sem + i for i in range(n)},
        compiler_params=pltpu.CompilerParams(has_side_effects=EFFECT),
    )(*[_hbm(l) for l in lands])
    return list(out[:2 * n_sem]), list(out[2 * n_sem:2 * n_sem + n]), out[-1]


def _forward_wait(sems, lands, after, name):
    n = len(lands)
    n_sem = 3 * n

    def body(*refs):
        lnd = refs[:n]
        s_sems, r_sems = refs[n:n + n_sem], refs[n + n_sem:n + 2 * n_sem]
        x, y, c, chips = _place()
        for k in range(n):
            for j, (px, py) in enumerate(chips):
                cp = pltpu.make_async_remote_copy(
                    src_ref=_half_rows(lnd[k], 2 * px + py, c), dst_ref=_half_rows(lnd[k], 2 * px + py, 1 - c),
                    send_sem=s_sems[3 * k + j], recv_sem=r_sems[3 * k + j],
                    device_id=(x, y, 1 - c), device_id_type=MESH)
                cp.wait_send()
                cp.wait_recv()

    out = pl.pallas_call(
        body, name=name,
        out_shape=tuple(pltpu.HBM(l.shape, l.dtype) for l in lands),
        in_specs=[HBM_SPEC] * n + [SEM_SPEC] * (2 * n_sem) + [ANY_SPEC] * len(after),
        out_specs=[HBM_SPEC] * n,
        input_output_aliases={i: i for i in range(n)},
        compiler_params=pltpu.CompilerParams(has_side_effects=EFFECT),
    )(*lands, *sems, *after)
    return list(out)


def _blocks_finish(lands, name):
    n = len(lands)

    def body(*refs):
        lnd = refs[n:2 * n]
        send_sems, recv_sems = refs[2 * n:]
        x, y, c, chips = _place()
        sibling = (x, y, 1 - c)

        def copy(k, j, chip_id, pc):
            return pltpu.make_async_remote_copy(
                src_ref=_half_rows(lnd[k], chip_id, pc), dst_ref=_half_rows(lnd[k], chip_id, pc),
                send_sem=send_sems.at[k, j], recv_sem=recv_sems.at[k, j], device_id=sibling, device_id_type=MESH)

        started = []
        for k in range(n):
            for j, (px, py) in enumerate(chips):
                cp = copy(k, j, 2 * px + py, c)
                cp.start()
                started.append(cp)
        for k in range(n):
            for j, (px, py) in enumerate(chips):
                copy(k, j, 2 * px + py, 1 - c).wait_recv()
        for cp in started:
            cp.wait_send()

    out = pl.pallas_call(
        body, name=name,
        out_shape=[jax.ShapeDtypeStruct(l.shape, l.dtype) for l in lands],
        in_specs=[ANY_SPEC] * n, out_specs=[ANY_SPEC] * n,
        input_output_aliases={i: i for i in range(n)},
        scratch_shapes=[pltpu.SemaphoreType.DMA((n, 3)), pltpu.SemaphoreType.DMA((n, 3))],
    )(*lands)
    return list(out)


def _gather_start(shards, name):
    n = len(shards)
    n_sem = 3 * n

    def body(*refs):
        ins, lands = refs[:n], refs[n:2 * n]
        send_sems, recv_sems = refs[2 * n:2 * n + n_sem], refs[2 * n + n_sem:2 * n + 2 * n_sem]
        token = refs[-1]
        x, y, c, chips = _place()
        me_chip = 2 * x + y
        for k in range(n):
            h = shards[k].shape[0] // 2
            rows = pl.ds(pl.multiple_of(c * h, 8), h)
            for j, chip in enumerate(chips):
                pltpu.make_async_remote_copy(
                    src_ref=ins[k].at[rows, :], dst_ref=lands[k].at[me_chip, rows, :],
                    send_sem=send_sems[3 * k + j], recv_sem=recv_sems[3 * k + j],
                    device_id=(*chip, c), device_id_type=MESH).start()
        token[...] = jnp.zeros_like(token)

    lands = [_hbm(lax.empty((N_CHIP,) + s.shape, s.dtype)) for s in shards]
    out = pl.pallas_call(
        body, name=name,
        out_shape=(*[pltpu.SemaphoreType.DMA(())] * (2 * n_sem),
                   *[pltpu.HBM(s.shape, s.dtype) for s in shards],
                   *[pltpu.HBM(l.shape, l.dtype) for l in lands],
                   jax.ShapeDtypeStruct((8, 128), F32)),
        in_specs=[HBM_SPEC] * (2 * n),
        out_specs=(*[SEM_SPEC] * (2 * n_sem), *[HBM_SPEC] * (2 * n), VMEM_SPEC),
        input_output_aliases={i: 2 * n_sem + i for i in range(2 * n)},
        compiler_params=pltpu.CompilerParams(has_side_effects=EFFECT),
    )(*[_hbm(s) for s in shards], *lands)
    sems = list(out[:2 * n_sem])
    return sems, list(out[2 * n_sem:2 * n_sem + n]), list(out[2 * n_sem + n:2 * n_sem + 2 * n]), out[-1]


def _gather_wait(sems, shards, lands, after, name):
    n = len(shards)
    n_sem = 3 * n

    def body(*refs):
        ins, lnd = refs[:n], refs[n:2 * n]
        s_sems, r_sems = refs[2 * n:2 * n + n_sem], refs[2 * n + n_sem:2 * n + 2 * n_sem]
        x, y, c, chips = _place()
        for k in range(n):
            h = shards[k].shape[0] // 2
            rows = pl.ds(pl.multiple_of(c * h, 8), h)
            for j, (px, py) in enumerate(chips):
                cp = pltpu.make_async_remote_copy(
                    src_ref=ins[k].at[rows, :], dst_ref=lnd[k].at[2 * px + py, rows, :],
                    send_sem=s_sems[3 * k + j], recv_sem=r_sems[3 * k + j],
                    device_id=(px, py, c), device_id_type=MESH)
                cp.wait_send()
                cp.wait_recv()

    out = pl.pallas_call(
        body, name=name,
        out_shape=(*[pltpu.HBM(s.shape, s.dtype) for s in shards], *[pltpu.HBM(l.shape, l.dtype) for l in lands]),
        in_specs=[HBM_SPEC] * (2 * n) + [SEM_SPEC] * (2 * n_sem) + [ANY_SPEC],
        out_specs=[HBM_SPEC] * (2 * n),
        input_output_aliases={i: i for i in range(2 * n)},
        compiler_params=pltpu.CompilerParams(has_side_effects=EFFECT),
    )(*shards, *lands, *sems, after)
    return list(out[:n]), list(out[n:])


def _gather_finish(shards, lands, name):
    n = len(shards)

    def body(*refs):
        ins, lnd = refs[:n], refs[2 * n:3 * n]
        send_sems, recv_sems, local_sems = refs[3 * n:]
        x, y, c, chips = _place()
        me_chip = 2 * x + y
        sibling = (x, y, 1 - c)

        def half(k, chip_id, pc):
            h = shards[k].shape[0] // 2
            return lnd[k].at[chip_id, pl.ds(pl.multiple_of(pc * h, 8), h), :]

        def copy(k, j, chip_id, pc):
            return pltpu.make_async_remote_copy(
                src_ref=half(k, chip_id, pc), dst_ref=half(k, chip_id, pc),
                send_sem=send_sems.at[k, j], recv_sem=recv_sems.at[k, j], device_id=sibling, device_id_type=MESH)

        locals_, started = [], []
        for k in range(n):
            cp = pltpu.make_async_copy(ins[k], lnd[k].at[me_chip], local_sems.at[k])
            cp.start()
            locals_.append(cp)
            for j, (px, py) in enumerate(chips):
                cp = copy(k, j, 2 * px + py, c)
                cp.start()
                started.append(cp)
        for k in range(n):
            for j, (px, py) in enumerate(chips):
                copy(k, j, 2 * px + py, 1 - c).wait_recv()
        for cp in started:
            cp.wait_send()
        for cp in locals_:
            cp.wait()

    out = pl.pallas_call(
        body, name=name,
        out_shape=[jax.ShapeDtypeStruct(l.shape, l.dtype) for l in lands],
        in_specs=[ANY_SPEC] * (2 * n), out_specs=[ANY_SPEC] * n,
        input_output_aliases={n + i: i for i in range(n)},
        scratch_shapes=[pltpu.SemaphoreType.DMA((n, 3)), pltpu.SemaphoreType.DMA((n, 3)),
                        pltpu.SemaphoreType.DMA((n,))],
    )(*shards, *lands)
    return list(out)


def _send_other_half(arrs, name):
    n = len(arrs)

    def body(*refs):
        ins, outs = refs[:n], refs[n:2 * n]
        send_sems, recv_sems = refs[2 * n:]
        x, y, c, _ = _place()
        cps = []
        for k in range(n):
            cp = pltpu.make_async_remote_copy(
                src_ref=_half_of(ins[k], 1 - c), dst_ref=outs[k],
                send_sem=send_sems.at[k], recv_sem=recv_sems.at[k], device_id=(x, y, 1 - c), device_id_type=MESH)
            cp.start()
            cps.append(cp)
        for cp in cps:
            cp.wait()

    return pl.pallas_call(
        body, name=name,
        out_shape=[jax.ShapeDtypeStruct(_half_shape(a.shape), a.dtype) for a in arrs],
        in_specs=[ANY_SPEC] * n, out_specs=[ANY_SPEC] * n,
        scratch_shapes=[pltpu.SemaphoreType.DMA((n,)), pltpu.SemaphoreType.DMA((n,))],
    )(*arrs)


def _send_half_start(arrs, name):
    n = len(arrs)

    def body(*refs):
        ins, lnd = refs[:n], refs[n:2 * n]
        send_sems, recv_sems = refs[2 * n:3 * n], refs[3 * n:4 * n]
        token = refs[-1]
        x, y, c, _ = _place()
        for k in range(n):
            pltpu.make_async_remote_copy(
                src_ref=_half_of(ins[k], 1 - c), dst_ref=lnd[k], send_sem=send_sems[k], recv_sem=recv_sems[k],
                device_id=(x, y, 1 - c), device_id_type=MESH).start()
        token[...] = jnp.zeros_like(token)

    lands = [_hbm(lax.empty(_half_shape(a.shape), a.dtype)) for a in arrs]
    out = pl.pallas_call(
        body, name=name,
        out_shape=(*[pltpu.SemaphoreType.DMA(())] * (2 * n), *[pltpu.HBM(a.shape, a.dtype) for a in arrs],
                   *[pltpu.HBM(l.shape, l.dtype) for l in lands], jax.ShapeDtypeStruct((8, 128), F32)),
        in_specs=[HBM_SPEC] * (2 * n),
        out_specs=(*[SEM_SPEC] * (2 * n), *[HBM_SPEC] * (2 * n), VMEM_SPEC),
        input_output_aliases={i: 2 * n + i for i in range(2 * n)},
        compiler_params=pltpu.CompilerParams(has_side_effects=EFFECT),
    )(*[_hbm(a) for a in arrs], *lands)
    return list(out[:2 * n]), list(out[2 * n:3 * n]), list(out[3 * n:4 * n]), out[-1]


def _send_half_wait(sems, arrs, lands, after, name):
    n = len(arrs)

    def body(*refs):
        ins, lnd = refs[:n], refs[n:2 * n]
        s_sems, r_sems = refs[2 * n:3 * n], refs[3 * n:4 * n]
        x, y, c, _ = _place()
        for k in range(n):
            cp = pltpu.make_async_remote_copy(
                src_ref=_half_of(ins[k], 1 - c), dst_ref=lnd[k], send_sem=s_sems[k], recv_sem=r_sems[k],
                device_id=(x, y, 1 - c), device_id_type=MESH)
            cp.wait_send()
            cp.wait_recv()

    out = pl.pallas_call(
        body, name=name,
        out_shape=tuple(pltpu.HBM(a.shape, a.dtype) for a in list(arrs) + list(lands)),
        in_specs=[HBM_SPEC] * (2 * n) + [SEM_SPEC] * (2 * n) + [ANY_SPEC] * len(after),
        out_specs=[HBM_SPEC] * (2 * n),
        input_output_aliases={i: i for i in range(2 * n)},
        compiler_params=pltpu.CompilerParams(has_side_effects=EFFECT),
    )(*arrs, *lands, *sems, *after)
    return list(out[:n]), list(out[n:])


def _blocks_to_owner(arrs, name):
    n = len(arrs)

    def body(*refs):
        ins, outs = refs[:n], refs[n:2 * n]
        send_sems, recv_sems, local_sems = refs[2 * n:]
        x, y, c, chips = _place()
        me_chip = 2 * x + y
        locals_, started = [], []
        for k in range(n):
            cp = pltpu.make_async_copy(ins[k].at[me_chip], outs[k].at[me_chip], local_sems.at[k])
            cp.start()
            locals_.append(cp)

        def copy(k, j, src_block, dst_slot, to):
            return pltpu.make_async_remote_copy(
                src_ref=ins[k].at[src_block], dst_ref=outs[k].at[dst_slot],
                send_sem=send_sems.at[k, j], recv_sem=recv_sems.at[k, j], device_id=to, device_id_type=MESH)

        for k in range(n):
            for j, (px, py) in enumerate(chips):
                cp = copy(k, j, 2 * px + py, me_chip, (px, py, c))
                cp.start()
                started.append(cp)
        for k in range(n):
            for j, (px, py) in enumerate(chips):
                copy(k, j, me_chip, 2 * px + py, (px, py, c)).wait_recv()
        for cp in started:
            cp.wait_send()
        for cp in locals_:
            cp.wait()

    return pl.pallas_call(
        body, name=name,
        out_shape=[jax.ShapeDtypeStruct(a.shape, a.dtype) for a in arrs],
        in_specs=[ANY_SPEC] * n, out_specs=[ANY_SPEC] * n,
        scratch_shapes=[pltpu.SemaphoreType.DMA((n, 3)), pltpu.SemaphoreType.DMA((n, 3)),
                        pltpu.SemaphoreType.DMA((n,))],
    )(*arrs)


def _scatter_blocks(arrs, name):
    n = len(arrs)

    def body(*refs):
        ins, outs = refs[:n], refs[n:2 * n]
        send_sems, recv_sems = refs[2 * n:]
        x, y, c, chips = _place()
        me_chip = 2 * x + y

        def copy(k, j, src_block, dst_slot, to):
            return pltpu.make_async_remote_copy(
                src_ref=ins[k].at[src_block], dst_ref=outs[k].at[dst_slot],
                send_sem=send_sems.at[k, j], recv_sem=recv_sems.at[k, j], device_id=to, device_id_type=MESH)

        started = []
        for k in range(n):
            for j, (px, py) in enumerate(chips):
                cp = copy(k, j, 2 * px + py, me_chip, (px, py, c))
                cp.start()
                started.append(cp)
        for k in range(n):
            for j, (px, py) in enumerate(chips):
                copy(k, j, me_chip, 2 * px + py, (px, py, c)).wait_recv()
        for cp in started:
            cp.wait_send()

    return pl.pallas_call(
        body, name=name,
        out_shape=[jax.ShapeDtypeStruct(a.shape, a.dtype) for a in arrs],
        in_specs=[ANY_SPEC] * n, out_specs=[ANY_SPEC] * n,
        scratch_shapes=[pltpu.SemaphoreType.DMA((n, 3)), pltpu.SemaphoreType.DMA((n, 3))],
    )(*arrs)


def _scatter_start(arrs, name, after=()):
    n = len(arrs)
    n_sem = 3 * n
    first = 2 * n + len(after)

    def body(*refs):
        ins, lnd = refs[:n], refs[n:2 * n]
        send_sems, recv_sems = refs[first:first + n_sem], refs[first + n_sem:first + 2 * n_sem]
        token = refs[-1]
        x, y, c, chips = _place()
        me_chip = 2 * x + y
        for k in range(n):
            for j, (px, py) in enumerate(chips):
                pltpu.make_async_remote_copy(
                    src_ref=ins[k].at[2 * px + py], dst_ref=lnd[k].at[me_chip],
                    send_sem=send_sems[3 * k + j], recv_sem=recv_sems[3 * k + j],
                    device_id=(px, py, c), device_id_type=MESH).start()
        token[...] = jnp.zeros_like(token)

    lands = [_hbm(lax.empty(a.shape, a.dtype)) for a in arrs]
    out = pl.pallas_call(
        body, name=name,
        out_shape=(*[pltpu.SemaphoreType.DMA(())] * (2 * n_sem),
                   *[pltpu.HBM(a.shape, a.dtype) for a in arrs], *[pltpu.HBM(a.shape, a.dtype) for a in arrs],
                   jax.ShapeDtypeStruct((8, 128), F32)),
        in_specs=[HBM_SPEC] * (2 * n) + [ANY_SPEC] * len(after),
        out_specs=(*[SEM_SPEC] * (2 * n_sem), *[HBM_SPEC] * (2 * n), VMEM_SPEC),
        input_output_aliases={i: 2 * n_sem + i for i in range(2 * n)},
        compiler_params=pltpu.CompilerParams(has_side_effects=EFFECT),
    )(*[_hbm(a) for a in arrs], *lands, *after)
    base = 2 * n_sem
    return list(out[:base]), list(out[base:base + n]), list(out[base + n:base + 2 * n]), out[-1]


def _scatter_wait(sems, arrs, lands, after, name):
    n = len(arrs)
    n_sem = 3 * n

    def body(*refs):
        ins, lnd = refs[:n], refs[n:2 * n]
        s_sems, r_sems = refs[2 * n:2 * n + n_sem], refs[2 * n + n_sem:2 * n + 2 * n_sem]
        x, y, c, chips = _place()
        for k in range(n):
            for j, (px, py) in enumerate(chips):
                cp = pltpu.make_async_remote_copy(
                    src_ref=ins[k].at[2 * px + py], dst_ref=lnd[k].at[2 * px + py],
                    send_sem=s_sems[3 * k + j], recv_sem=r_sems[3 * k + j],
                    device_id=(px, py, c), device_id_type=MESH)
                cp.wait_send()
                cp.wait_recv()

    out = pl.pallas_call(
        body, name=name,
        out_shape=tuple(pltpu.HBM(a.shape, a.dtype) for a in list(arrs) + list(lands)),
        in_specs=[HBM_SPEC] * (2 * n) + [SEM_SPEC] * (2 * n_sem) + [ANY_SPEC] * len(after),
        out_specs=[HBM_SPEC] * (2 * n),
        input_output_aliases={i: i for i in range(2 * n)},
        compiler_params=pltpu.CompilerParams(has_side_effects=EFFECT),
    )(*arrs, *lands, *sems, *after)
    return list(out[:n]), list(out[n:])


def _sum_owner(chip_arr, pairs, got, name):
    nb, h, cols = got.shape
    tr = _row_tile(h, 16, 256)

    def body(chip_ref, own_ref, a_ref, b_ref, c_ref, o_ref):
        o_ref[...] = ((own_ref[0].astype(F32) + a_ref[0].astype(F32)) + b_ref[0].astype(F32)) + c_ref[0].astype(F32)

    def slot(off):
        return pl.BlockSpec((1, tr, cols), lambda i, chip_ref: ((chip_ref[0] + off) % N_CHIP, i, 0))

    return pl.pallas_call(
        body, name=name,
        grid_spec=pltpu.PrefetchScalarGridSpec(
            num_scalar_prefetch=1, grid=(h // tr,),
            in_specs=[slot(0), slot(1), slot(2), slot(3)],
            out_specs=pl.BlockSpec((tr, cols), lambda i, chip_ref: (i, 0))),
        out_shape=jax.ShapeDtypeStruct((h, cols), F32),
        compiler_params=_cparams(dimension_semantics=("parallel",)),
    )(chip_arr, pairs, got, got, got)


def _swap_start(arrs, name, after=()):
    n = len(arrs)
    first = 2 * n + len(after)

    def body(*refs):
        ins, lnd = refs[:n], refs[n:2 * n]
        send_sems, recv_sems = refs[first:first + n], refs[first + n:first + 2 * n]
        x, y, c, _ = _place()
        for k in range(n):
            pltpu.make_async_remote_copy(
                src_ref=ins[k], dst_ref=lnd[k], send_sem=send_sems[k], recv_sem=recv_sems[k],
                device_id=(x, y, 1 - c), device_id_type=MESH).start()

    lands = [_hbm(lax.empty(a.shape, a.dtype)) for a in arrs]
    out = pl.pallas_call(
        body, name=name,
        out_shape=(*[pltpu.SemaphoreType.DMA(())] * (2 * n), *[pltpu.HBM(a.shape, a.dtype) for a in arrs],
                   *[pltpu.HBM(a.shape, a.dtype) for a in arrs]),
        in_specs=[HBM_SPEC] * (2 * n) + [ANY_SPEC] * len(after),
        out_specs=(*[SEM_SPEC] * (2 * n), *[HBM_SPEC] * (2 * n)),
        input_output_aliases={i: 2 * n + i for i in range(2 * n)},
        compiler_params=pltpu.CompilerParams(has_side_effects=EFFECT),
    )(*[_hbm(a) for a in arrs], *lands, *after)
    return list(out[:2 * n]), list(out[2 * n:3 * n]), list(out[3 * n:4 * n])


def _swap_wait(sems, arrs, lands, after, name):
    n = len(arrs)

    def body(*refs):
        ins, lnd = refs[:n], refs[n:2 * n]
        s_sems, r_sems = refs[2 * n:3 * n], refs[3 * n:4 * n]
        x, y, c, _ = _place()
        for k in range(n):
            cp = pltpu.make_async_remote_copy(
                src_ref=ins[k], dst_ref=lnd[k], send_sem=s_sems[k], recv_sem=r_sems[k],
                device_id=(x, y, 1 - c), device_id_type=MESH)
            cp.wait_send()
            cp.wait_recv()

    out = pl.pallas_call(
        body, name=name,
        out_shape=tuple(pltpu.HBM(a.shape, a.dtype) for a in list(arrs) + list(lands)),
        in_specs=[HBM_SPEC] * (2 * n) + [SEM_SPEC] * (2 * n) + [ANY_SPEC] * len(after),
        out_specs=[HBM_SPEC] * (2 * n),
        input_output_aliases={i: i for i in range(2 * n)},
        compiler_params=pltpu.CompilerParams(has_side_effects=EFFECT),
    )(*arrs, *lands, *sems, *after)
    return list(out[:n]), list(out[n:])


def _swap_with_sibling(arrs, name):
    n = len(arrs)

    def body(*refs):
        ins, outs = refs[:n], refs[n:2 * n]
        send_sems, recv_sems = refs[2 * n:]
        x, y, c, _ = _place()
        cps = []
        for k in range(n):
            cp = pltpu.make_async_remote_copy(
                src_ref=ins[k], dst_ref=outs[k], send_sem=send_sems.at[k], recv_sem=recv_sems.at[k],
                device_id=(x, y, 1 - c), device_id_type=MESH)
            cp.start()
            cps.append(cp)
        for cp in cps:
            cp.wait()

    return pl.pallas_call(
        body, name=name,
        out_shape=[jax.ShapeDtypeStruct(a.shape, a.dtype) for a in arrs],
        in_specs=[ANY_SPEC] * n, out_specs=[ANY_SPEC] * n,
        scratch_shapes=[pltpu.SemaphoreType.DMA((n,)), pltpu.SemaphoreType.DMA((n,))],
    )(*arrs)


def _row_tile(h, mult=8, cap=128):
    for t in range(cap - cap % mult, mult - 1, -mult):
        if h % t == 0:
            return t
    if mult > 8:
        return _row_tile(h, 8, cap)
    raise ValueError(h)


def _cast_bf16(a, name):
    rows, cols = a.shape
    tr = _row_tile(rows, 16, 256)

    def body(a_ref, o_ref):
        o_ref[...] = a_ref[...].astype(BF16)

    return pl.pallas_call(
        body, name=name, grid=(rows // tr,),
        out_shape=jax.ShapeDtypeStruct(a.shape, BF16),
        in_specs=[pl.BlockSpec((tr, cols), lambda i: (i, 0))],
        out_specs=pl.BlockSpec((tr, cols), lambda i: (i, 0)),
        compiler_params=_cparams(dimension_semantics=("parallel",)),
    )(a)


def _pair_sum(c_arr, full, recv, name):
    nb, rows, cols = full.shape

    def body(c_ref, f_ref, r_ref, o_ref):
        o_ref[...] = (f_ref[...] + r_ref[...]).astype(BF16)

    if _halved_by_rows(full.shape):
        h = rows // 2
        tr = _row_tile(h, 16, 256)
        steps = h // tr
        own = pl.BlockSpec((1, tr, cols), lambda b, i, c_ref: (b, c_ref[0] * steps + i, 0))
        half = pl.BlockSpec((1, tr, cols), lambda b, i, c_ref: (b, i, 0))
    else:
        steps = 1
        own = pl.BlockSpec((1, rows, cols // 2), lambda b, i, c_ref: (b, 0, c_ref[0]))
        half = pl.BlockSpec((1, rows, cols // 2), lambda b, i, c_ref: (b, 0, 0))
    return pl.pallas_call(
        body, name=name,
        grid_spec=pltpu.PrefetchScalarGridSpec(
            num_scalar_prefetch=1, grid=(nb, steps), in_specs=[own, half], out_specs=half),
        out_shape=jax.ShapeDtypeStruct(_half_shape(full.shape), BF16),
        compiler_params=_cparams(dimension_semantics=("parallel", "parallel")),
    )(c_arr, full, recv)


def _sum_chips(got, name):
    nb, h, cols = got.shape
    tr = _row_tile(h, 16, 256)

    def body(g_ref, o_ref):
        g = g_ref[...].astype(F32)
        o_ref[...] = ((g[0] + g[1]) + g[2]) + g[3]

    return pl.pallas_call(
        body, name=name, grid=(h // tr,),
        out_shape=jax.ShapeDtypeStruct((h, cols), F32),
        in_specs=[pl.BlockSpec((nb, tr, cols), lambda i: (0, i, 0))],
        out_specs=pl.BlockSpec((tr, cols), lambda i: (i, 0)),
        compiler_params=_cparams(dimension_semantics=("parallel",)),
    )(got)


def _adam_math(g, w, m, v):
    m1 = ADAM_B1 * m + (1.0 - ADAM_B1) * g
    v1 = ADAM_B2 * v + (1.0 - ADAM_B2) * (g * g)
    m_hat = m1 / (1.0 - ADAM_B1 ** ADAM_STEP)
    v_hat = v1 / (1.0 - ADAM_B2 ** ADAM_STEP)
    delta = -ADAM_LR * (m_hat / (jnp.sqrt(v_hat) + ADAM_EPS) + ADAM_WD * w)
    return delta, m1, v1


def _adamw_halves(c_arr, own, other, w, m, v, name):
    rows, cols = w.shape
    by_rows = own.shape[1] == cols

    def body(c_ref, own_ref, oth_ref, w_ref, m_ref, v_ref, g_out, d_out, m_out, v_out):
        if by_rows:
            g = jnp.where(pl.program_id(0) == c_ref[0], own_ref[...], oth_ref[...])
        else:
            own_, oth_ = own_ref[...], oth_ref[...]
            g = jnp.where(c_ref[0] == 0, jnp.concatenate([own_, oth_], axis=1), jnp.concatenate([oth_, own_], axis=1))
        d, m1, v1 = _adam_math(g, w_ref[...], m_ref[...], v_ref[...])
        g_out[...] = g
        d_out[...] = d
        m_out[...] = m1
        v_out[...] = v1

    if by_rows:
        h = rows // 2
        tr = _row_tile(h)
        steps = h // tr
        grid = (2, steps)
        half_spec = pl.BlockSpec((tr, cols), lambda p, i, c_ref: (i, 0))
        full_spec = pl.BlockSpec((tr, cols), lambda p, i, c_ref: (p * steps + i, 0))
    else:
        tr = _row_tile(rows)
        grid = (1, rows // tr)
        half_spec = pl.BlockSpec((tr, cols // 2), lambda p, i, c_ref: (i, 0))
        full_spec = pl.BlockSpec((tr, cols), lambda p, i, c_ref: (i, 0))
    return pl.pallas_call(
        body, name=name,
        grid_spec=pltpu.PrefetchScalarGridSpec(
            num_scalar_prefetch=1, grid=grid,
            in_specs=[half_spec, half_spec, full_spec, full_spec, full_spec],
            out_specs=[full_spec] * 4),
        out_shape=[jax.ShapeDtypeStruct(w.shape, F32)] * 4,
        compiler_params=_cparams(dimension_semantics=("parallel", "parallel")),
    )(c_arr, own, other, w, m, v)


def _adamw_whole(items, name):
    n = len(items)

    def body(*refs):
        ins, outs = refs[:4 * n], refs[4 * n:]
        for k in range(n):
            g, w, m, v = (r[...] for r in ins[4 * k:4 * k + 4])
            d, m1, v1 = _adam_math(g, w, m, v)
            outs[3 * k][...] = d
            outs[3 * k + 1][...] = m1
            outs[3 * k + 2][...] = v1

    flat = [a for it in items for a in it]
    shapes = [jax.ShapeDtypeStruct(it[1].shape, F32) for it in items for _ in range(3)]
    out = pl.pallas_call(
        body, name=name, out_shape=shapes,
        in_specs=[VMEM_SPEC] * (4 * n), out_specs=[VMEM_SPEC] * (3 * n),
        compiler_params=_cparams(),
    )(*flat)
    return [tuple(out[3 * k:3 * k + 3]) for k in range(n)]


def _adamw_tiled(g, w, m, v, name):
    rows, cols = w.shape
    tr = _row_tile(rows)

    def body(g_ref, w_ref, m_ref, v_ref, d_out, m_out, v_out):
        d, m1, v1 = _adam_math(g_ref[...], w_ref[...], m_ref[...], v_ref[...])
        d_out[...] = d
        m_out[...] = m1
        v_out[...] = v1

    spec = pl.BlockSpec((tr, cols), lambda i: (i, 0))
    return pl.pallas_call(
        body, name=name, grid=(rows // tr,),
        out_shape=[jax.ShapeDtypeStruct(w.shape, F32)] * 3,
        in_specs=[spec] * 4, out_specs=[spec] * 3,
        compiler_params=_cparams(dimension_semantics=("parallel",)),
    )(g, w, m, v)


def _mod_forward(cond, w_mod, b_mod_cols, name):
    def body(c_ref, w_ref, b_ref, o_ref):
        o_ref[...] = _dot(_silu(c_ref[...]), w_ref[...]) + b_ref[...]

    return pl.pallas_call(
        body, name=name, out_shape=jax.ShapeDtypeStruct((cond.shape[0], w_mod.shape[1]), F32),
        in_specs=[VMEM_SPEC] * 3, out_specs=VMEM_SPEC, compiler_params=_cparams(),
    )(cond, w_mod, b_mod_cols)


def _mod_backward(cond, w_mod, dmod_cols, name):
    def body(c_ref, w_ref, d_ref, gw_ref, gc_ref):
        s = _silu(c_ref[...])
        d = d_ref[...]
        gw_ref[...] = _dot_tn(s, d)
        gc_ref[...] = _dot_nt(d[8:16, :], w_ref[...])

    return pl.pallas_call(
        body, name=name,
        out_shape=[jax.ShapeDtypeStruct(w_mod.shape, F32), jax.ShapeDtypeStruct((8, w_mod.shape[0]), F32)],
        in_specs=[VMEM_SPEC] * 3, out_specs=[VMEM_SPEC] * 2, compiler_params=_cparams(),
    )(cond, w_mod, dmod_cols)


def _col_chunks(width, step=512):
    return [(s, min(step, width - s)) for s in range(0, width, step)]


def _w_in_row(p_off):
    if p_off < 9 * HW:
        return p_off
    return 9 * HW if p_off == OFF_LR else p_off + 2 * RANK


def _in_projection_cols(ctx0, x0, modc, modx, pre1, w_t, name):
    d = x0.shape[1]
    n_ctx, n_lat = ctx0.shape[0], x0.shape[0]
    rows = n_ctx + n_lat
    n_main = OFF_LR // HW

    def modulated(z, mod_ref, pre_ref):
        n, _ = _rms(z)
        return (n * pre_ref[...] * (1.0 + mod_ref[1:2, :]) + mod_ref[0:1, :]).astype(BF16)

    def body(ctx_ref, x_ref, modc_ref, modx_ref, pre_ref, w_ref, h_ref, p_ref):
        j = pl.program_id(0)

        @pl.when(j == 0)
        def _():
            h_ref[0:n_ctx, :] = modulated(ctx_ref[...], modc_ref, pre_ref)
            for r0 in range(0, n_lat, TM):
                h_ref[n_ctx + r0:n_ctx + r0 + TM, :] = modulated(x_ref[r0:r0 + TM, :], modx_ref, pre_ref)

        row = pl.multiple_of(jnp.where(j < 9, j * HW, j * HW + 2 * RANK), 32)
        p_ref[...] = _dot_nt(h_ref[...], w_ref[pl.ds(row, HW), :])

    fixed = lambda j: (0, 0)
    h1, p = pl.pallas_call(
        body, name=name, grid=(n_main,),
        out_shape=[jax.ShapeDtypeStruct((rows, d), BF16), jax.ShapeDtypeStruct((rows, P_WIDTH), F32)],
        in_specs=[VMEM_SPEC, VMEM_SPEC, pl.BlockSpec((8, d), fixed), pl.BlockSpec((8, d), fixed),
                  pl.BlockSpec((1, d), fixed), VMEM_SPEC],
        out_specs=[pl.BlockSpec((rows, d), fixed), pl.BlockSpec((rows, HW), lambda j: (0, j))],
        compiler_params=_cparams(dimension_semantics=("arbitrary",)),
    )(ctx0, x0, modc, modx, pre1, w_t)

    lr0 = _w_in_row(OFF_LR)

    def lr_body(h_ref, w_ref, p_in, p_ref):
        p_ref[...] = _dot_nt(h_ref[...], w_ref[...])

    p = pl.pallas_call(
        lr_body, name=name + "_lr", grid=(1,),
        out_shape=jax.ShapeDtypeStruct((rows, P_WIDTH), F32),
        in_specs=[pl.BlockSpec((rows, d), lambda j: (0, 0)), pl.BlockSpec((128, d), lambda j: (lr0 // 128, 0)),
                  ANY_SPEC],
        out_specs=pl.BlockSpec((rows, 128), lambda j: (0, OFF_LR // 128)),
        input_output_aliases={2: 0},
        compiler_params=_cparams(),
    )(h1, w_t, p)
    return h1, p


def _in_projection(ctx0, x0, modc, modx, pre1, w_t, n_ctx_tiles, name):
    d = x0.shape[1]
    rows = ctx0.shape[0] + x0.shape[0]
    width = P_WIDTH

    def body(ctx_ref, x_ref, modc_ref, modx_ref, pre_ref, w_ref, h_ref, p_ref):
        is_ctx = pl.program_id(0) < n_ctx_tiles
        n, _ = _rms(jnp.where(is_ctx, ctx_ref[...], x_ref[...]))
        shift = jnp.where(is_ctx, modc_ref[0:1, :], modx_ref[0:1, :])
        scale = jnp.where(is_ctx, modc_ref[1:2, :], modx_ref[1:2, :])
        h = (n * pre_ref[...] * (1.0 + scale) + shift).astype(BF16)
        h_ref[...] = h
        for s, w in _col_chunks(width):
            p_ref[:, s:s + w] = _dot_nt(h, w_ref[_w_in_row(s):_w_in_row(s) + w, :])

    row = lambda i: (i, 0)
    fixed = lambda i: (0, 0)
    return pl.pallas_call(
        body, name=name, grid=(rows // TM,),
        out_shape=[jax.ShapeDtypeStruct((rows, d), BF16), jax.ShapeDtypeStruct((rows, width), F32)],
        in_specs=[pl.BlockSpec((TM, d), lambda i: (jnp.minimum(i, n_ctx_tiles - 1), 0)),
                  pl.BlockSpec((TM, d), lambda i: (jnp.maximum(i - n_ctx_tiles, 0), 0)),
                  pl.BlockSpec((8, d), fixed), pl.BlockSpec((8, d), fixed), pl.BlockSpec((1, d), fixed), VMEM_SPEC],
        out_specs=[pl.BlockSpec((TM, d), row), pl.BlockSpec((TM, width), row)],
        compiler_params=_cparams(dimension_semantics=("parallel",)),
    )(ctx0, x0, modc, modx, pre1, w_t)


C_HQ, C_HI, C_HF_FW, C_HF_BW, C_HGATE, C_GQ, C_GK, C_GV, C_GGATE = range(9)
OFF_GATE_HG = 9 * HW
OFF_LR = 13 * HW
P_WIDTH = OFF_LR + 128


def _head_norm_fwd(o, w):
    outs, ns, rs = [], [], []
    for h in range(NH):
        n, r = _rms(o[:, h * HD:(h + 1) * HD])
        ns.append(n)
        rs.append(r)
        outs.append(n * w)
    return jnp.concatenate(outs, axis=1), ns, rs


def _mixer_tail(z, o_hg, o_gla, p_hgate, p_ggate, p_gate_hg, p_gate_gla, hg_on, gla_on, wbh, wbg, wout):
    on_hg, n_hg, r_hg = _head_norm_fwd(o_hg, hg_on)
    on_gla, n_gla, r_gla = _head_norm_fwd(o_gla, gla_on)
    og_hg = (on_hg * _silu(p_hgate)).astype(BF16)
    og_gla = (on_gla * _silu(p_ggate)).astype(BF16)
    b_hg = jnp.dot(og_hg, wbh, preferred_element_type=F32)
    b_gla = jnp.dot(og_gla, wbg, preferred_element_type=F32)
    s_hg = _sigmoid(p_gate_hg)
    s_gla = _sigmoid(p_gate_gla)
    merged = (s_hg * b_hg + s_gla * b_gla).astype(BF16)
    y1 = jnp.dot(merged, wout, preferred_element_type=F32)
    return dict(on_hg=on_hg, n_hg=n_hg, r_hg=r_hg, on_gla=on_gla, n_gla=n_gla, r_gla=r_gla, og_hg=og_hg,
                og_gla=og_gla, b_hg=b_hg, b_gla=b_gla, s_hg=s_hg, s_gla=s_gla, merged=merged, y1=y1)


def _mixer_ffn(x_lat, p, o_list, modx, norms, onorms, w_br_hg, w_br_gla, w_out, w_gate, w_up, w_down, target,
               n_ctx_tiles, name):
    rows, d = x_lat.shape
    dff = w_gate.shape[0]
    inv_d = 1.0 / d

    def body(x_ref, ofw_hg, obw_hg, ofw_gla, obw_gla, p_hgate, p_ggate, p_ghg_a, p_ghg_b, p_ggla_a, p_ggla_b,
             modx_ref, norm_ref, on_ref, wbh_ref, wbg_ref, wout_ref, wg_ref, wu_ref, wd_ref, t_ref,
             loss_ref, dz2_ref, y1_ref, mrg_ref, oghg_ref, oggla_ref, h2_ref, a_ref, du_ref, dv_ref, dy2_ref,
             stat_ref):
        i = pl.program_id(0)
        post1, pre2, post2 = norm_ref[1:2, :], norm_ref[2:3, :], norm_ref[3:4, :]
        gate1, shift2, scale2, gate2 = modx_ref[2:3, :], modx_ref[3:4, :], modx_ref[4:5, :], modx_ref[5:6, :]
        p_gate_hg = jnp.concatenate([p_ghg_a[...], p_ghg_b[...]], axis=1)
        p_gate_gla = jnp.concatenate([p_ggla_a[...], p_ggla_b[...]], axis=1)
        t = _mixer_tail(x_ref[...], ofw_hg[...] + obw_hg[...], ofw_gla[...] + obw_gla[...], p_hgate[...],
                        p_ggate[...], p_gate_hg, p_gate_gla, on_ref[0:1, 0:HD], on_ref[1:2, 0:HD],
                        wbh_ref[...], wbg_ref[...], wout_ref[...])
        y1_ref[...] = t["y1"]
        mrg_ref[...] = t["merged"]
        oghg_ref[...] = t["og_hg"]
        oggla_ref[...] = t["og_gla"]
        n1, _ = _rms(t["y1"])
        z2 = x_ref[...] + n1 * post1 * gate1
        n2, r2 = _rms(z2)
        nw2 = n2 * pre2
        h2 = (nw2 * (1.0 + scale2) + shift2).astype(BF16)
        h2_ref[...] = h2
        u = _dot_nt(h2, wg_ref[...])
        v = _dot_nt(h2, wu_ref[...])
        su = _silu(u)
        a = (su * v).astype(BF16)
        a_ref[...] = a
        y2 = jnp.dot(a, wd_ref[...], preferred_element_type=F32)
        n3, r3 = _rms(y2)
        z3 = z2 + n3 * post2 * gate2
        err = z3 - t_ref[...]
        part = 0.5 * inv_d * jnp.sum(err * err)
        dz3 = err * inv_d
        dgate2 = _colsum(dz3 * n3 * post2)
        tt = dz3 * gate2
        dpost2 = _colsum(tt * n3)
        dy2 = _rms_bwd(tt * post2, n3, r3).astype(BF16)
        dy2_ref[...] = dy2
        da = _dot_nt(dy2, wd_ref[...])
        du = (da * v * _dsilu(u)).astype(BF16)
        dv = (da * su).astype(BF16)
        du_ref[...] = du
        dv_ref[...] = dv
        dh2 = (jnp.dot(du, wg_ref[...], preferred_element_type=F32)
               + jnp.dot(dv, wu_ref[...], preferred_element_type=F32))
        dshift2 = _colsum(dh2)
        dscale2 = _colsum(dh2 * nw2)
        dnw2 = dh2 * (1.0 + scale2)
        dpre2 = _colsum(dnw2 * n2)
        dz2_ref[...] = dz3 + _rms_bwd(dnw2 * pre2, n2, r2)

        @pl.when(i == 0)
        def _():
            stat_ref[...] = jnp.zeros_like(stat_ref)
            loss_ref[...] = jnp.zeros_like(loss_ref)

        for r, val in enumerate((dshift2, dscale2, dgate2, dpre2, dpost2)):
            stat_ref[r:r + 1, :] += val
        loss_ref[...] += part
        stat_ref[5:6, 0:128] += part

    tm = TM_FFN
    ctx_tiles = n_ctx_tiles * (TM // tm)
    lat = lambda i: (i, 0)
    full = lambda i: (i + ctx_tiles, 0)
    fixed = lambda i: (0, 0)

    def pcol(blk):
        return pl.BlockSpec((tm, HW), lambda i: (i + ctx_tiles, blk))

    in_specs = ([pl.BlockSpec((tm, d), lat)] + [pl.BlockSpec((tm, HW), full)] * 4
                + [pcol(C_HGATE), pcol(C_GGATE), pcol(9), pcol(10), pcol(11), pcol(12)]
                + [pl.BlockSpec((8, d), fixed), pl.BlockSpec((8, d), fixed), pl.BlockSpec((8, d), fixed)]
                + [VMEM_SPEC] * 6 + [pl.BlockSpec((tm, d), lat)])
    bf = lambda w: jax.ShapeDtypeStruct((rows, w), BF16)
    out_shape = [jax.ShapeDtypeStruct((8, 128), F32), jax.ShapeDtypeStruct((rows, d), F32),
                 jax.ShapeDtypeStruct((rows, d), F32), bf(d), bf(HW), bf(HW), bf(d), bf(dff), bf(dff), bf(dff), bf(d),
                 jax.ShapeDtypeStruct((8, d), F32)]
    out_specs = [pl.BlockSpec((8, 128), fixed), pl.BlockSpec((tm, d), lat), pl.BlockSpec((tm, d), lat),
                 pl.BlockSpec((tm, d), lat), pl.BlockSpec((tm, HW), lat), pl.BlockSpec((tm, HW), lat),
                 pl.BlockSpec((tm, d), lat), pl.BlockSpec((tm, dff), lat), pl.BlockSpec((tm, dff), lat),
                 pl.BlockSpec((tm, dff), lat), pl.BlockSpec((tm, d), lat), pl.BlockSpec((8, d), fixed)]
    return pl.pallas_call(
        body, name=name, grid=(rows // tm,), out_shape=out_shape, in_specs=in_specs, out_specs=out_specs,
        compiler_params=_cparams(dimension_semantics=("arbitrary",)),
    )(x_lat, *o_list, p, p, p, p, p, p, modx, norms, onorms, w_br_hg, w_br_gla, w_out, w_gate, w_up, w_down, target)


def _mixer_tail_fwd(x_lat, p, o_list, modx, norms, onorms, w_br_hg, w_br_gla, w_out, n_ctx_tiles, name):
    rows, d = x_lat.shape

    def body(x_ref, ofw_hg, obw_hg, ofw_gla, obw_gla, p_hgate, p_ggate, p_ghg_a, p_ghg_b, p_ggla_a, p_ggla_b,
             modx_ref, norm_ref, on_ref, wbh_ref, wbg_ref, wout_ref, z2_ref, y1_ref, mrg_ref, oghg_ref, oggla_ref):
        p_gate_hg = jnp.concatenate([p_ghg_a[...], p_ghg_b[...]], axis=1)
        p_gate_gla = jnp.concatenate([p_ggla_a[...], p_ggla_b[...]], axis=1)
        t = _mixer_tail(x_ref[...], ofw_hg[...] + obw_hg[...], ofw_gla[...] + obw_gla[...], p_hgate[...],
                        p_ggate[...], p_gate_hg, p_gate_gla, on_ref[0:1, 0:HD], on_ref[1:2, 0:HD],
                        wbh_ref[...], wbg_ref[...], wout_ref[...])
        y1_ref[...] = t["y1"]
        mrg_ref[...] = t["merged"]
        oghg_ref[...] = t["og_hg"]
        oggla_ref[...] = t["og_gla"]
        n1, _ = _rms(t["y1"])
        z2_ref[...] = x_ref[...] + n1 * norm_ref[1:2, :] * modx_ref[2:3, :]

    lat = lambda i: (i, 0)
    full = lambda i: (i + n_ctx_tiles, 0)
    fixed = lambda i: (0, 0)

    def pcol(blk):
        return pl.BlockSpec((TM, HW), lambda i: (i + n_ctx_tiles, blk))

    in_specs = ([pl.BlockSpec((TM, d), lat)] + [pl.BlockSpec((TM, HW), full)] * 4
                + [pcol(C_HGATE), pcol(C_GGATE), pcol(9), pcol(10), pcol(11), pcol(12)]
                + [pl.BlockSpec((8, d), fixed)] * 3 + [VMEM_SPEC] * 3)
    bf = lambda w: jax.ShapeDtypeStruct((rows, w), BF16)
    f32 = jax.ShapeDtypeStruct((rows, d), F32)
    return pl.pallas_call(
        body, name=name, grid=(rows // TM,), out_shape=[f32, f32, bf(d), bf(HW), bf(HW)], in_specs=in_specs,
        out_specs=[pl.BlockSpec((TM, d), lat)] * 3 + [pl.BlockSpec((TM, HW), lat)] * 2,
        compiler_params=_cparams(dimension_semantics=("parallel",)),
    )(x_lat, *o_list, p, p, p, p, p, p, modx, norms, onorms, w_br_hg, w_br_gla, w_out)


def _ffn_fwd_bwd(z2, modx, norms, w_gate, w_up, w_down, target, name):
    rows, d = z2.shape
    dff = w_gate.shape[0]
    inv_d = 1.0 / d

    def body(z2_ref, modx_ref, norm_ref, wg_ref, wu_ref, wd_ref, t_ref,
             loss_ref, dz2_ref, h2_ref, a_ref, du_ref, dv_ref, dy2_ref, stat_ref):
        i = pl.program_id(0)
        pre2, post2 = norm_ref[2:3, :], norm_ref[3:4, :]
        shift2, scale2, gate2 = modx_ref[3:4, :], modx_ref[4:5, :], modx_ref[5:6, :]
        z2 = z2_ref[...]
        n2, r2 = _rms(z2)
        nw2 = n2 * pre2
        h2 = (nw2 * (1.0 + scale2) + shift2).astype(BF16)
        h2_ref[...] = h2
        u = _dot_nt(h2, wg_ref[...])
        v = _dot_nt(h2, wu_ref[...])
        su = _silu(u)
        a = (su * v).astype(BF16)
        a_ref[...] = a
        y2 = jnp.dot(a, wd_ref[...], preferred_element_type=F32)
        n3, r3 = _rms(y2)
        err = z2 + n3 * post2 * gate2 - t_ref[...]
        part = 0.5 * inv_d * jnp.sum(err * err)
        dz3 = err * inv_d
        dgate2 = _colsum(dz3 * n3 * post2)
        tt = dz3 * gate2
        dpost2 = _colsum(tt * n3)
        dy2 = _rms_bwd(tt * post2, n3, r3).astype(BF16)
        dy2_ref[...] = dy2
        da = _dot_nt(dy2, wd_ref[...])
        du = (da * v * _dsilu(u)).astype(BF16)
        dv = (da * su).astype(BF16)
        du_ref[...] = du
        dv_ref[...] = dv
        dh2 = (jnp.dot(du, wg_ref[...], preferred_element_type=F32)
               + jnp.dot(dv, wu_ref[...], preferred_element_type=F32))
        dshift2 = _colsum(dh2)
        dscale2 = _colsum(dh2 * nw2)
        dnw2 = dh2 * (1.0 + scale2)
        dpre2 = _colsum(dnw2 * n2)
        dz2_ref[...] = dz3 + _rms_bwd(dnw2 * pre2, n2, r2)

        @pl.when(i == 0)
        def _():
            stat_ref[...] = jnp.zeros_like(stat_ref)
            loss_ref[...] = jnp.zeros_like(loss_ref)

        for r, val in enumerate((dshift2, dscale2, dgate2, dpre2, dpost2)):
            stat_ref[r:r + 1, :] += val
        loss_ref[...] += part
        stat_ref[5:6, 0:128] += part

    lat = lambda i: (i, 0)
    fixed = lambda i: (0, 0)
    bf = lambda w: jax.ShapeDtypeStruct((rows, w), BF16)
    return pl.pallas_call(
        body, name=name, grid=(rows // TM,),
        out_shape=[jax.ShapeDtypeStruct((8, 128), F32), jax.ShapeDtypeStruct((rows, d), F32), bf(d), bf(dff), bf(dff),
                   bf(dff), bf(d), jax.ShapeDtypeStruct((8, d), F32)],
        in_specs=[pl.BlockSpec((TM, d), lat), pl.BlockSpec((8, d), fixed), pl.BlockSpec((8, d), fixed)]
        + [VMEM_SPEC] * 3 + [pl.BlockSpec((TM, d), lat)],
        out_specs=[pl.BlockSpec((8, 128), fixed), pl.BlockSpec((TM, d), lat), pl.BlockSpec((TM, d), lat),
                   pl.BlockSpec((TM, dff), lat), pl.BlockSpec((TM, dff), lat), pl.BlockSpec((TM, dff), lat),
                   pl.BlockSpec((TM, d), lat), pl.BlockSpec((8, d), fixed)],
        compiler_params=_cparams(dimension_semantics=("arbitrary",)),
    )(z2, modx, norms, w_gate, w_up, w_down, target)


def _mixer_tail_bwd(x_lat, p, o_list, dz2, y1, modx, norms, onorms, w_br_hg, w_br_gla, w_out, n_ctx_tiles, n_tiles,
                    name):
    rows, d = x_lat.shape
    total = n_tiles * TM

    def body(x_ref, ofw_hg, obw_hg, ofw_gla, obw_gla, p_hgate, p_ggate, p_ghg_a, p_ghg_b, p_ggla_a, p_ggla_b,
             dz2_ref, y1_ref, modx_ref, norm_ref, on_ref, wbh_ref, wbg_ref, wout_ref,
             dohg_ref, dogla_ref, dhgate_ref, dggate_ref, dghg_ref, dggla_ref, dy1_ref, dbhg_ref, dbgla_ref,
             stat_ref):
        i = pl.program_id(0)

        @pl.when(i == 0)
        def _():
            stat_ref[...] = jnp.zeros_like(stat_ref)

        @pl.when(i < n_ctx_tiles)
        def _():
            for ref in (dohg_ref, dogla_ref, dhgate_ref, dggate_ref, dghg_ref, dggla_ref):
                ref[...] = jnp.zeros_like(ref)

        @pl.when(i >= n_ctx_tiles)
        def _():
            post1, gate1 = norm_ref[1:2, :], modx_ref[2:3, :]
            hg_on, gla_on = on_ref[0:1, 0:HD], on_ref[1:2, 0:HD]
            p_gate_hg = jnp.concatenate([p_ghg_a[...], p_ghg_b[...]], axis=1)
            p_gate_gla = jnp.concatenate([p_ggla_a[...], p_ggla_b[...]], axis=1)
            ph, pg = p_hgate[...], p_ggate[...]
            t = _mixer_tail(x_ref[...], ofw_hg[...] + obw_hg[...], ofw_gla[...] + obw_gla[...], ph, pg,
                            p_gate_hg, p_gate_gla, hg_on, gla_on, wbh_ref[...], wbg_ref[...], wout_ref[...])
            dz2 = dz2_ref[...]
            n1, r1 = _rms(y1_ref[...])
            dgate1 = _colsum(dz2 * n1 * post1)
            tt = dz2 * gate1
            dpost1 = _colsum(tt * n1)
            dy1 = _rms_bwd(tt * post1, n1, r1).astype(BF16)
            dy1_ref[...] = dy1
            dmerged = _dot_nt(dy1, wout_ref[...])
            dghg_ref[...] = (dmerged * t["b_hg"] * t["s_hg"] * (1.0 - t["s_hg"])).astype(BF16)
            dggla_ref[...] = (dmerged * t["b_gla"] * t["s_gla"] * (1.0 - t["s_gla"])).astype(BF16)
            db_hg = (dmerged * t["s_hg"]).astype(BF16)
            db_gla = (dmerged * t["s_gla"]).astype(BF16)
            dbhg_ref[...] = db_hg
            dbgla_ref[...] = db_gla
            don_acc = []
            for (db, wb, pgate, on, ns, rs, gain, gate_ref, do_ref) in (
                    (db_hg, wbh_ref, ph, t["on_hg"], t["n_hg"], t["r_hg"], hg_on, dhgate_ref, dohg_ref),
                    (db_gla, wbg_ref, pg, t["on_gla"], t["n_gla"], t["r_gla"], gla_on, dggate_ref, dogla_ref)):
                dog = _dot_nt(db, wb[...])
                gate_ref[...] = (dog * on * _dsilu(pgate)).astype(BF16)
                don = dog * _silu(pgate)
                acc = jnp.zeros((1, HD), F32)
                for h in range(NH):
                    sl = slice(h * HD, (h + 1) * HD)
                    acc = acc + _colsum(don[:, sl] * ns[h])
                    do_ref[:, sl] = _rms_bwd(don[:, sl] * gain, ns[h], rs[h]).astype(BF16)
                don_acc.append(acc)
            stat_ref[0:1, :] += dgate1
            stat_ref[1:2, :] += dpost1
            stat_ref[2:3, 0:HD] += don_acc[0]
            stat_ref[2:3, HD:2 * HD] += don_acc[1]

    lat = lambda i: (jnp.maximum(i - n_ctx_tiles, 0), 0)
    full = lambda i: (i, 0)
    fixed = lambda i: (0, 0)

    def pcol(blk):
        return pl.BlockSpec((TM, HW), lambda i: (i, blk))

    in_specs = ([pl.BlockSpec((TM, d), lat)] + [pl.BlockSpec((TM, HW), full)] * 4
                + [pcol(C_HGATE), pcol(C_GGATE), pcol(9), pcol(10), pcol(11), pcol(12)]
                + [pl.BlockSpec((TM, d), lat), pl.BlockSpec((TM, d), lat)]
                + [pl.BlockSpec((8, d), fixed)] * 3 + [VMEM_SPEC] * 3)
    f = lambda w: jax.ShapeDtypeStruct((total, w), BF16)
    out_shape = [f(HW), f(HW), f(HW), f(HW), f(d), f(d), jax.ShapeDtypeStruct((rows, d), BF16),
                 jax.ShapeDtypeStruct((rows, d), BF16), jax.ShapeDtypeStruct((rows, d), BF16),
                 jax.ShapeDtypeStruct((8, d), F32)]
    out_specs = ([pl.BlockSpec((TM, HW), full)] * 4 + [pl.BlockSpec((TM, d), full)] * 2
                 + [pl.BlockSpec((TM, d), lat)] * 3 + [pl.BlockSpec((8, d), fixed)])
    return pl.pallas_call(
        body, name=name, grid=(n_tiles,), out_shape=out_shape, in_specs=in_specs, out_specs=out_specs,
        compiler_params=_cparams(dimension_semantics=("arbitrary",)),
    )(x_lat, *o_list, p, p, p, p, p, p, dz2, y1, modx, norms, onorms, w_br_hg, w_br_gla, w_out)


def _in_projection_bwd(ctx0, x0, dz2, modc, modx, pre1, w_t, pieces, n_ctx_tiles, name):
    d = x0.shape[1]
    rows = ctx0.shape[0] + x0.shape[0]
    lat_rows = dz2.shape[0]
    width = P_WIDTH
    n_pieces = len(pieces)

    def body(*refs):
        ctx_ref, x_ref, dz2_ref, modc_ref, modx_ref, pre_ref, w_ref = refs[:7]
        (dhq_f, dhq_b, dhi_f, dhi_b, dhf_f, dhf_b, dhgate, dgq_f, dgq_b, dgk_f, dgk_b, dgv_f, dgv_b, dggate,
         dghg, dggla, dlr_f, dlr_b) = refs[7:7 + n_pieces]
        dp_ref, gx_ref, stat_ref = refs[7 + n_pieces:]
        i = pl.program_id(0)
        is_ctx = i < n_ctx_tiles
        z = jnp.where(is_ctx, ctx_ref[...], x_ref[...])
        sections = [
            (0, dhq_f[...] + dhq_b[...]), (HW, dhi_f[...] + dhi_b[...]), (2 * HW, dhf_f[...]), (3 * HW, dhf_b[...]),
            (4 * HW, dhgate[...]), (5 * HW, dgq_f[...] + dgq_b[...]), (6 * HW, dgk_f[...] + dgk_b[...]),
            (7 * HW, dgv_f[...] + dgv_b[...]), (8 * HW, dggate[...]),
            (9 * HW, dghg[:, 0:HW]), (10 * HW, dghg[:, HW:2 * HW]),
            (11 * HW, dggla[:, 0:HW]), (12 * HW, dggla[:, HW:2 * HW]), (OFF_LR, dlr_f[...] + dlr_b[...])]
        dh = jnp.zeros((TM, d), F32)
        for off, val in sections:
            w = val.shape[1]
            vb = val.astype(BF16)
            dp_ref[:, off:off + w] = vb
            dh = dh + jnp.dot(vb, w_ref[_w_in_row(off):_w_in_row(off) + w, :], preferred_element_type=F32)
        n, r = _rms(z)
        pre = pre_ref[...]
        scale = jnp.where(is_ctx, modc_ref[1:2, :], modx_ref[1:2, :])
        nw = n * pre
        dshift = _colsum(dh)
        dscale = _colsum(dh * nw)
        dnw = dh * (1.0 + scale)
        dpre = _colsum(dnw * n)
        gx_ref[...] = dz2_ref[...] + _rms_bwd(dnw * pre, n, r)
        zero = jnp.zeros((1, d), F32)

        @pl.when(i == 0)
        def _():
            stat_ref[...] = jnp.zeros_like(stat_ref)

        stat_ref[0:1, :] += jnp.where(is_ctx, zero, dshift)
        stat_ref[1:2, :] += jnp.where(is_ctx, zero, dscale)
        stat_ref[2:3, :] += jnp.where(is_ctx, dshift, zero)
        stat_ref[3:4, :] += jnp.where(is_ctx, dscale, zero)
        stat_ref[4:5, :] += dpre

    full = lambda i: (i, 0)
    lat = lambda i: (jnp.maximum(i - n_ctx_tiles, 0), 0)
    fixed = lambda i: (0, 0)
    piece_specs = [pl.BlockSpec((TM, a.shape[1]), full) for a in pieces]
    in_specs = [pl.BlockSpec((TM, d), lambda i: (jnp.minimum(i, n_ctx_tiles - 1), 0)), pl.BlockSpec((TM, d), lat),
                pl.BlockSpec((TM, d), lat), pl.BlockSpec((8, d), fixed),
                pl.BlockSpec((8, d), fixed), pl.BlockSpec((1, d), fixed), VMEM_SPEC] + piece_specs
    return pl.pallas_call(
        body, name=name, grid=(rows // TM,),
        out_shape=[jax.ShapeDtypeStruct((rows, width), BF16), jax.ShapeDtypeStruct((lat_rows, d), F32),
                   jax.ShapeDtypeStruct((8, d), F32)],
        in_specs=in_specs,
        out_specs=[pl.BlockSpec((TM, width), full), pl.BlockSpec((TM, d), lat), pl.BlockSpec((8, d), fixed)],
        compiler_params=_cparams(dimension_semantics=("arbitrary",)),
    )(ctx0, x0, dz2, modc, modx, pre1, w_t, *pieces)


def _transposed_lhs_matmul(x_ref, dy_ref, o_ref, xt_ref):
    @pl.when(pl.program_id(1) == 0)
    def _():
        xt_ref[...] = x_ref[...].T

    o_ref[...] = jnp.dot(xt_ref[...], dy_ref[...], preferred_element_type=F32)


def _w_in_grad(dp, h1, n_cols, name):
    rows, d = h1.shape
    n_main = OFF_LR // HW
    lr0 = _w_in_row(OFF_LR)

    def body(x_ref, xlr_ref, h_ref, o_hbm, xt_ref, acc_ref, sem):
        i = pl.program_id(0)

        def main_copy(step):
            row = jnp.where(step < 9, step * HW, step * HW + 2 * RANK)
            return pltpu.make_async_copy(acc_ref, o_hbm.at[pl.ds(pl.multiple_of(row, 8), HW), :], sem)

        lr_copy = pltpu.make_async_copy(acc_ref.at[0:2 * RANK, :], o_hbm.at[lr0:lr0 + 2 * RANK, :], sem)

        @pl.when(i < n_main)
        def _():
            xt_ref[...] = x_ref[...].T

        @pl.when(i > 0)
        def _():
            main_copy(i - 1).wait()

        @pl.when(i < n_main)
        def _():
            acc_ref[...] = jnp.dot(xt_ref[...], h_ref[...], preferred_element_type=F32)
            main_copy(i).start()

        @pl.when(i == n_main)
        def _():
            xt_ref[0:128, :] = xlr_ref[...].T
            acc_ref[0:128, :] = jnp.dot(xt_ref[0:128, :], h_ref[...], preferred_element_type=F32)
            lr_copy.start()
            lr_copy.wait()

    return pl.pallas_call(
        body, name=name, grid=(n_main + 1,),
        out_shape=jax.ShapeDtypeStruct((n_cols, d), F32),
        in_specs=[pl.BlockSpec((rows, HW), lambda i: (0, jnp.minimum(i, n_main - 1))),
                  pl.BlockSpec((rows, 128), lambda i: (0, OFF_LR // 128)),
                  pl.BlockSpec((rows, d), lambda i: (0, 0))],
        out_specs=ANY_SPEC,
        scratch_shapes=[pltpu.VMEM((HW, rows), BF16), pltpu.VMEM((HW, d), F32), pltpu.SemaphoreType.DMA],
        compiler_params=_cparams(dimension_semantics=("arbitrary",)),
    )(dp, dp, h1)


def _weight_grad(xs, dy, name, tk=None, tn=512, k_first=0, k_tiles=None):
    rows = dy.shape[0]
    n = dy.shape[1]
    tn_ = min(tn, n)
    tk_ = xs.shape[1] if tk is None else tk
    k_tiles = xs.shape[1] // tk_ if k_tiles is None else k_tiles
    k = k_tiles * tk_

    return pl.pallas_call(
        functools.partial(_transposed_lhs_matmul), name=name, grid=(k_tiles, n // tn_),
        out_shape=jax.ShapeDtypeStruct((k, n), F32),
        in_specs=[pl.BlockSpec((rows, tk_), lambda i, j: (0, i + k_first)),
                  pl.BlockSpec((rows, tn_), lambda i, j: (0, j))],
        out_specs=pl.BlockSpec((tk_, tn_), lambda i, j: (i, j)),
        scratch_shapes=[pltpu.VMEM((tk_, rows), BF16)],
        compiler_params=_cparams(dimension_semantics=("parallel", "arbitrary")),
    )(xs, dy)


def _running_sum(x, fw):
    c = x.shape[0]
    row = lax.broadcasted_iota(jnp.int32, (c, 1), 0)
    s = 1
    while s < c:
        if fw:
            x = x + jnp.where(row >= s, pltpu.roll(x, s, axis=0), 0.0)
        else:
            x = x + jnp.where(row < c - s, pltpu.roll(x, c - s, axis=0), 0.0)
        s *= 2
    return x


def _chunk_terms(q, k, g, fw):
    c = CHUNK
    r = lax.broadcasted_iota(jnp.int32, (c, c), 0)
    s = lax.broadcasted_iota(jnp.int32, (c, c), 1)
    causal = (s <= r) if fw else (s >= r)
    causal_t = (s >= r) if fw else (s <= r)
    cum = _running_sum(g, fw)
    row = lax.broadcasted_iota(jnp.int32, (c, 1), 0)
    pos = row if fw else (c - 1 - row)
    starts = [None]
    for j in range(1, NSUB):
        rj = SUB * j - 1 if fw else c - SUB * j
        starts.append(cum[rj:rj + 1, :])
    in_blk = [(pos >= SUB * j) & (pos < SUB * (j + 1)) for j in range(NSUB)]
    e = [jnp.exp(cum)]
    for j in range(1, NSUB):
        e.append(jnp.exp(jnp.where(pos >= SUB * j, cum - starts[j], -1e30)))
    own = jnp.zeros_like(cum)
    for j in range(1, NSUB):
        own = own + jnp.where(in_blk[j], starts[j], 0.0)
    kscale = jnp.exp(own - cum)
    rend = c - 1 if fw else 0
    cend = cum[rend:rend + 1, :]
    tail = jnp.exp(cend - cum)
    qcat = jnp.concatenate([q * e[j] for j in range(NSUB)], axis=1).astype(BF16)
    kt = k * kscale
    km = jnp.concatenate([jnp.where(in_blk[j], kt, 0.0) for j in range(NSUB)], axis=1).astype(BF16)
    return dict(causal=causal, causal_t=causal_t, e=e, in_blk=in_blk, kscale=kscale, cend=cend, tail=tail,
                qcat=qcat, km=km)


def _chunk_fwd(q, k, v, g, st0, fw):
    t = _chunk_terms(q, k, g, fw)
    a = jnp.where(t["causal"], _dot_nt(t["qcat"], t["km"]), 0.0)
    o = _dot(a, v) + _dot_nt(t["qcat"][:, 0:HD], st0)
    st1 = st0 * jnp.exp(t["cend"]) + _dot_tn(v, k * t["tail"])
    return o, st1


def _chunk_bwd(q, k, v, g, st0, do, dst1, fw):
    t = _chunk_terms(q, k, g, fw)
    qcat, km, e = t["qcat"], t["km"], t["e"]
    a_t = jnp.where(t["causal_t"], _dot_nt(km, qcat), 0.0)
    ktail = k * t["tail"]
    dv = _dot(a_t, do) + _dot_nt(ktail, dst1)
    da = jnp.where(t["causal"], _dot_nt(do, v), 0.0)
    da_t = jnp.where(t["causal_t"], _dot_nt(v, do), 0.0)
    dqcat = _dot(da, km)
    dq_inter = e[0] * _dot(do, st0)
    dq = dq_inter
    for j in range(NSUB):
        dq = dq + e[j] * dqcat[:, j * HD:(j + 1) * HD]
    dkm = _dot(da_t, qcat)
    dkt = jnp.zeros_like(k)
    for j in range(NSUB):
        dkt = dkt + jnp.where(t["in_blk"][j], dkm[:, j * HD:(j + 1) * HD], 0.0)
    dk_inter = _dot(v, dst1) * t["tail"]
    dk = dkt * t["kscale"] + dk_inter
    dcum = q * dq_inter - k * dk_inter
    for j in range(NSUB):
        sl = slice(j * HD, (j + 1) * HD)
        dcum = dcum + qcat[:, sl].astype(F32) * dqcat[:, sl] - km[:, sl].astype(F32) * dkm[:, sl]
    ecend = jnp.exp(t["cend"])
    end = ecend * _colsum(st0 * dst1) + _colsum(k * dk_inter)
    dg = _running_sum(dcum, not fw) + end
    dst0 = dst1 * ecend + _dot_tn(do, q * e[0])
    return dq, dk, dv, dg, dst0


def _running_sums(xs, fws):
    c = xs[0].shape[0]
    row = lax.broadcasted_iota(jnp.int32, (c, 1), 0)
    s = 1
    while s < c:
        xs = [x + (jnp.where(row >= s, pltpu.roll(x, s, axis=0), 0.0) if fw else
                   jnp.where(row < c - s, pltpu.roll(x, c - s, axis=0), 0.0)) for x, fw in zip(xs, fws)]
        s *= 2
    return xs


def _chunks_terms(qs, ks, gs, fws):
    c = CHUNK
    n = len(qs)
    r = lax.broadcasted_iota(jnp.int32, (c, c), 0)
    s = lax.broadcasted_iota(jnp.int32, (c, c), 1)
    row = lax.broadcasted_iota(jnp.int32, (c, 1), 0)
    per_dir = {}
    for fw in set(fws):
        pos = row if fw else (c - 1 - row)
        per_dir[fw] = dict(
            causal=(s <= r) if fw else (s >= r), causal_t=(s >= r) if fw else (s <= r), pos=pos,
            in_blk=[(pos >= SUB * j) & (pos < SUB * (j + 1)) for j in range(NSUB)],
            start_row=[None] + [SUB * j - 1 if fw else c - SUB * j for j in range(1, NSUB)],
            rend=c - 1 if fw else 0)
    dirs = [per_dir[fw] for fw in fws]
    cums = _running_sums(gs, fws)
    starts = [[None] + [cum[d["start_row"][j]:d["start_row"][j] + 1, :] for j in range(1, NSUB)]
              for cum, d in zip(cums, dirs)]
    es = [[jnp.exp(cum) for cum in cums]]
    for j in range(1, NSUB):
        es.append([jnp.exp(jnp.where(d["pos"] >= SUB * j, cum - st[j], -1e30)) for cum, st, d in zip(cums, starts, dirs)])
    owns = [sum(jnp.where(d["in_blk"][j], st[j], 0.0) for j in range(1, NSUB)) for st, d in zip(starts, dirs)]
    kscales = [jnp.exp(own - cum) for own, cum in zip(owns, cums)]
    cends = [cum[d["rend"]:d["rend"] + 1, :] for cum, d in zip(cums, dirs)]
    tails = [jnp.exp(cend - cum) for cend, cum in zip(cends, cums)]
    qcats = [jnp.concatenate([q * es[j][i] for j in range(NSUB)], axis=1).astype(BF16) for i, q in enumerate(qs)]
    kts = [k * ksc for k, ksc in zip(ks, kscales)]
    kms = [jnp.concatenate([jnp.where(d["in_blk"][j], kt, 0.0) for j in range(NSUB)], axis=1).astype(BF16)
           for kt, d in zip(kts, dirs)]
    e_by_lane = [[es[j][i] for j in range(NSUB)] for i in range(n)]
    return dict(dirs=dirs, e=e_by_lane, kscale=kscales, cend=cends, tail=tails, qcat=qcats, km=kms)


def _chunks_fwd(qs, ks, vs, gs, st0s, fws):
    t = _chunks_terms(qs, ks, gs, fws)
    scores = [_dot_nt(qc, km) for qc, km in zip(t["qcat"], t["km"])]
    a = [jnp.where(d["causal"], sc, 0.0) for sc, d in zip(scores, t["dirs"])]
    inter = [_dot_nt(qc[:, 0:HD], st0) for qc, st0 in zip(t["qcat"], st0s)]
    intra = [_dot(a_, v) for a_, v in zip(a, vs)]
    os_ = [x + y for x, y in zip(intra, inter)]
    upd = [_dot_tn(v, k * tl) for v, k, tl in zip(vs, ks, t["tail"])]
    st1s = [st0 * jnp.exp(ce) + u for st0, ce, u in zip(st0s, t["cend"], upd)]
    return os_, st1s


def _chunks_bwd(qs, ks, vs, gs, st0s, dos, dst1s, fws):
    n = len(qs)
    t = _chunks_terms(qs, ks, gs, fws)
    qcat, km, e, dirs = t["qcat"], t["km"], t["e"], t["dirs"]
    a_t = [jnp.where(d["causal_t"], _dot_nt(km_, qc), 0.0) for km_, qc, d in zip(km, qcat, dirs)]
    ktail = [k * tl for k, tl in zip(ks, t["tail"])]
    dv_a = [_dot(at, do) for at, do in zip(a_t, dos)]
    dv_b = [_dot_nt(kt, ds) for kt, ds in zip(ktail, dst1s)]
    dv = [x + y for x, y in zip(dv_a, dv_b)]
    da = [jnp.where(d["causal"], _dot_nt(do, v), 0.0) for do, v, d in zip(dos, vs, dirs)]
    da_t = [jnp.where(d["causal_t"], _dot_nt(v, do), 0.0) for do, v, d in zip(dos, vs, dirs)]
    dqcat = [_dot(da_, km_) for da_, km_ in zip(da, km)]
    dq_inter = [e[i][0] * _dot(dos[i], st0s[i]) for i in range(n)]
    dkm = [_dot(dat, qc) for dat, qc in zip(da_t, qcat)]
    dk_inter = [_dot(v, ds) * tl for v, ds, tl in zip(vs, dst1s, t["tail"])]
    dq = [dq_inter[i] + sum(e[i][j] * dqcat[i][:, j * HD:(j + 1) * HD] for j in range(NSUB)) for i in range(n)]
    dkt = [sum(jnp.where(dirs[i]["in_blk"][j], dkm[i][:, j * HD:(j + 1) * HD], 0.0) for j in range(NSUB))
           for i in range(n)]
    dk = [dkt[i] * t["kscale"][i] + dk_inter[i] for i in range(n)]
    dcum = [qs[i] * dq_inter[i] - ks[i] * dk_inter[i]
            + sum(qcat[i][:, j * HD:(j + 1) * HD].astype(F32) * dqcat[i][:, j * HD:(j + 1) * HD]
                  - km[i][:, j * HD:(j + 1) * HD].astype(F32) * dkm[i][:, j * HD:(j + 1) * HD] for j in range(NSUB))
            for i in range(n)]
    ecend = [jnp.exp(ce) for ce in t["cend"]]
    end = [ecend[i] * _colsum(st0s[i] * dst1s[i]) + _colsum(ks[i] * dk_inter[i]) for i in range(n)]
    sums = _running_sums(dcum, [not fw for fw in fws])
    dg = [sm + en for sm, en in zip(sums, end)]
    upd = [_dot_tn(dos[i], qs[i] * e[i][0]) for i in range(n)]
    dst0 = [dst1s[i] * ecend[i] + upd[i] for i in range(n)]
    return dq, dk, dv, dg, dst0


def _chunk_index(step, n_ctx_chunks, n_chunks, fw):
    if fw:
        return step
    return jnp.where(step < n_ctx_chunks, n_ctx_chunks - 1 - step, n_chunks - 1 + n_ctx_chunks - step)


def _hg_inputs(hq, hf, lbv, d_idx, sl):
    lb = _sigmoid(lbv[d_idx:d_idx + 1, sl] - lbv[2 + d_idx:3 + d_idx, sl])
    sg = _sigmoid(hf)
    f = lb + (1.0 - lb) * sg
    return _silu(hq), 1.0 - f, jnp.log(f), f, sg, lb


def _scan_fwd(p, side, n_ctx_chunks, fw, branch, name):
    rows = p.shape[0]
    n_chunks = rows // CHUNK
    d_idx = 0 if fw else 1
    hg = branch == "hg"
    cols = (C_HQ, C_HI, C_HF_FW + d_idx) if hg else (C_GQ, C_GK, C_GV)

    def body(*refs):
        if hg:
            a_ref, b_ref, c_ref, lb_ref, o_ref, st_ref, state = refs
        else:
            a_ref, b_ref, c_ref, lr_ref, wgk_ref, bgk_ref, o_ref, st_ref, state = refs
            logits = _dot(lr_ref[...], wgk_ref[...]) + bgk_ref[...]
            g_all = _log_sigmoid(logits) * (1.0 / GATE_NORM)

        @pl.when(pl.program_id(0) == 0)
        def _():
            state[...] = jnp.zeros_like(state)

        for h in range(NH):
            sl = slice(h * HD, (h + 1) * HD)
            if hg:
                q, k, g, _, _, _ = _hg_inputs(a_ref[:, sl], c_ref[:, sl], lb_ref[...], d_idx, sl)
                v = b_ref[:, sl]
            else:
                q, k, v, g = a_ref[:, sl] * (HD ** -0.5), b_ref[:, sl], c_ref[:, sl], g_all[:, sl]
            st0 = state[h]
            st_ref[0, h] = st0
            o, st1 = _chunk_fwd(q, k, v, g, st0, fw)
            o_ref[:, sl] = o
            state[h] = st1

    def cmap(blk):
        return pl.BlockSpec((CHUNK, HW), lambda j: (_chunk_index(j, n_ctx_chunks, n_chunks, fw), blk))

    fixed = lambda j: (0, 0)
    in_specs = [cmap(cols[0]), cmap(cols[1]), cmap(cols[2])]
    if hg:
        in_specs += [pl.BlockSpec((4, HW), fixed)]
        args = (p, p, p, side)
    else:
        in_specs += [pl.BlockSpec((CHUNK, 128), lambda j: (_chunk_index(j, n_ctx_chunks, n_chunks, fw), OFF_LR // 128)),
                     pl.BlockSpec((128, HW), fixed), pl.BlockSpec((1, HW), fixed)]
        args = (p, p, p, p, side[0], side[1])
    return pl.pallas_call(
        body, name=name, grid=(n_chunks,),
        out_shape=[jax.ShapeDtypeStruct((rows, HW), F32), jax.ShapeDtypeStruct((n_chunks, NH, HD, HD), F32)],
        in_specs=in_specs,
        out_specs=[pl.BlockSpec((CHUNK, HW), lambda j: (_chunk_index(j, n_ctx_chunks, n_chunks, fw), 0)),
                   pl.BlockSpec((1, NH, HD, HD), lambda j: (_chunk_index(j, n_ctx_chunks, n_chunks, fw), 0, 0, 0))],
        scratch_shapes=[pltpu.VMEM((NH, HD, HD), F32)],
        compiler_params=_cparams(dimension_semantics=("arbitrary",)),
    )(*args)


def _scan_fwd_both(p, side, n_ctx_chunks, branch, name):
    rows = p.shape[0]
    n_chunks = rows // CHUNK
    hg = branch == "hg"
    n_in = 4 if hg else 6

    def body(*refs):
        ins, outs, state = refs[:2 * n_in], refs[2 * n_in:2 * n_in + 4], refs[-1]

        @pl.when(pl.program_id(0) == 0)
        def _():
            state[...] = jnp.zeros_like(state)

        lanes, where = [], []
        for di, fw in enumerate((True, False)):
            r = ins[di * n_in:(di + 1) * n_in]
            o_ref, st_ref = outs[2 * di], outs[2 * di + 1]
            if hg:
                a_ref, b_ref, c_ref, lb_ref = r
            else:
                a_ref, b_ref, c_ref, lr_ref, wgk_ref, bgk_ref = r
                logits = _dot(lr_ref[...], wgk_ref[...]) + bgk_ref[...]
                g_all = _log_sigmoid(logits) * (1.0 / GATE_NORM)
            for h in range(NH):
                sl = slice(h * HD, (h + 1) * HD)
                if hg:
                    q, k, g, _, _, _ = _hg_inputs(a_ref[:, sl], c_ref[:, sl], lb_ref[...], di, sl)
                    v = b_ref[:, sl]
                else:
                    q, k, v, g = a_ref[:, sl] * (HD ** -0.5), b_ref[:, sl], c_ref[:, sl], g_all[:, sl]
                lanes.append((q, k, v, g, state[di, h], fw))
                where.append((di, h, sl, o_ref, st_ref))
        qs, ks, vs, gs, st0s, fws = (list(col) for col in zip(*lanes))
        os_, st1s = _chunks_fwd(qs, ks, vs, gs, st0s, fws)
        for (di, h, sl, o_ref, st_ref), st0, o, st1 in zip(where, st0s, os_, st1s):
            st_ref[0, h] = st0
            o_ref[:, sl] = o
            state[di, h] = st1

    fixed = lambda j: (0, 0)
    in_specs, args, out_specs = [], [], []
    for di, fw in enumerate((True, False)):
        chunk = functools.partial(_chunk_index, n_ctx_chunks=n_ctx_chunks, n_chunks=n_chunks, fw=fw)

        def cmap(blk, width=HW, chunk=chunk):
            return pl.BlockSpec((CHUNK, width), lambda j: (chunk(j), blk))

        if hg:
            in_specs += [cmap(C_HQ), cmap(C_HI), cmap(C_HF_FW + di), pl.BlockSpec((4, HW), fixed)]
            args += [p, p, p, side]
        else:
            in_specs += [cmap(C_GQ), cmap(C_GK), cmap(C_GV), cmap(OFF_LR // 128, 128),
                         pl.BlockSpec((128, HW), fixed), pl.BlockSpec((1, HW), fixed)]
            args += [p, p, p, p, side[di][0], side[di][1]]
        out_specs += [cmap(0), pl.BlockSpec((1, NH, HD, HD), lambda j, chunk=chunk: (chunk(j), 0, 0, 0))]
    return pl.pallas_call(
        body, name=name, grid=(n_chunks,),
        out_shape=[jax.ShapeDtypeStruct((rows, HW), F32), jax.ShapeDtypeStruct((n_chunks, NH, HD, HD), F32)] * 2,
        in_specs=in_specs, out_specs=out_specs,
        scratch_shapes=[pltpu.VMEM((2, NH, HD, HD), F32)],
        compiler_params=_cparams(dimension_semantics=("arbitrary",)),
    )(*args)


def _scan_bwd_both(p, side, states, d_o, n_ctx_chunks, branch, name):
    rows = p.shape[0]
    n_chunks = rows // CHUNK
    hg = branch == "hg"
    n_in = 6 if hg else 8
    n_out = 4 if hg else 6

    def body(*refs):
        ins, outs, dstate = refs[:2 * n_in], refs[2 * n_in:2 * n_in + 2 * n_out], refs[-1]
        first = pl.program_id(0) == 0

        @pl.when(first)
        def _():
            dstate[...] = jnp.zeros_like(dstate)

        lanes, where, extra, ctx = [], [], [], []
        for di, fw in enumerate((True, False)):
            r, w = ins[di * n_in:(di + 1) * n_in], outs[di * n_out:(di + 1) * n_out]
            if hg:
                a_ref, b_ref, c_ref, lb_ref, st_ref, do_ref = r
                da_ref, db_ref, dc_ref, dlb_ref = w
                acc_refs = (dlb_ref,)
            else:
                a_ref, b_ref, c_ref, lr_ref, wgk_ref, bgk_ref, st_ref, do_ref = r
                da_ref, db_ref, dc_ref, dlr_ref, dwgk_ref, dbias_ref = w
                acc_refs = (dwgk_ref, dbias_ref)
                lr = lr_ref[...]
                logits = _dot(lr, wgk_ref[...]) + bgk_ref[...]
                g_all = _log_sigmoid(logits) * (1.0 / GATE_NORM)

            @pl.when(first)
            def _(acc_refs=acc_refs):
                for ref in acc_refs:
                    ref[...] = jnp.zeros_like(ref)

            for h in range(NH):
                sl = slice(h * HD, (h + 1) * HD)
                if hg:
                    hq, hf = a_ref[:, sl], c_ref[:, sl]
                    q, k, g, f, sg, lb = _hg_inputs(hq, hf, lb_ref[...], di, sl)
                    v = b_ref[:, sl]
                    extra.append((hq, f, sg, lb))
                else:
                    q, k, v, g = a_ref[:, sl] * (HD ** -0.5), b_ref[:, sl], c_ref[:, sl], g_all[:, sl]
                    extra.append(None)
                lanes.append((q, k, v, g, st_ref[0, h], do_ref[:, sl], dstate[di, h], fw))
                where.append((di, h, sl))
            ctx.append((w, None if hg else (lr, logits, wgk_ref)))

        qs, ks, vs, gs, st0s, dos, dst1s, fws = (list(col) for col in zip(*lanes))
        dqs, dks, dvs, dgs, dst0s = _chunks_bwd(qs, ks, vs, gs, st0s, dos, dst1s, fws)
        dg_parts = {0: [], 1: []}
        for (di, h, sl), ex, dq, dk, dv, dg, dst0 in zip(where, extra, dqs, dks, dvs, dgs, dst0s):
            dstate[di, h] = dst0
            w = ctx[di][0]
            if hg:
                hq, f, sg, lb = ex
                da_ref, db_ref, dc_ref, dlb_ref = w
                da_ref[:, sl] = (dq * _dsilu(hq)).astype(BF16)
                db_ref[:, sl] = dv.astype(BF16)
                df = dg / f - dk
                dc_ref[:, sl] = (df * (1.0 - lb) * sg * (1.0 - sg)).astype(BF16)
                dlb_ref[0:1, sl] += _colsum(df * (1.0 - sg))
            else:
                da_ref, db_ref, dc_ref = w[:3]
                da_ref[:, sl] = (dq * (HD ** -0.5)).astype(BF16)
                db_ref[:, sl] = dk.astype(BF16)
                dc_ref[:, sl] = dv.astype(BF16)
                dg_parts[di].append(dg)
        if not hg:
            for di in range(2):
                dlr_ref, dwgk_ref, dbias_ref = ctx[di][0][3:]
                lr, logits, wgk_ref = ctx[di][1]
                dlogits = jnp.concatenate(dg_parts[di], axis=1) * (1.0 / GATE_NORM) * (1.0 - _sigmoid(logits))
                dlr_ref[...] = _dot_nt(dlogits, wgk_ref[...]).astype(BF16)
                dwgk_ref[...] += _dot_tn(lr, dlogits)
                dbias_ref[0:1, :] += _colsum(dlogits)

    fixed = lambda j: (0, 0)
    big = jax.ShapeDtypeStruct((rows, HW), BF16)
    in_specs, args, out_shape, out_specs = [], [], [], []
    for di, fw in enumerate((True, False)):
        def chunk_of(j, fw=fw):
            return _chunk_index(n_chunks - 1 - j, n_ctx_chunks, n_chunks, fw)

        def cmap(blk, width=HW, chunk_of=chunk_of):
            return pl.BlockSpec((CHUNK, width), lambda j: (chunk_of(j), blk))

        st_spec = pl.BlockSpec((1, NH, HD, HD), lambda j, chunk_of=chunk_of: (chunk_of(j), 0, 0, 0))
        if hg:
            in_specs += [cmap(C_HQ), cmap(C_HI), cmap(C_HF_FW + di), pl.BlockSpec((4, HW), fixed), st_spec, cmap(0)]
            args += [p, p, p, side, states[di], d_o]
            out_shape += [big, big, big, jax.ShapeDtypeStruct((8, HW), F32)]
            out_specs += [cmap(0), cmap(0), cmap(0), pl.BlockSpec((8, HW), fixed)]
        else:
            in_specs += [cmap(C_GQ), cmap(C_GK), cmap(C_GV), cmap(OFF_LR // 128, 128),
                         pl.BlockSpec((128, HW), fixed), pl.BlockSpec((1, HW), fixed), st_spec, cmap(0)]
            args += [p, p, p, p, side[di][0], side[di][1], states[di], d_o]
            out_shape += [big, big, big, jax.ShapeDtypeStruct((rows, 128), BF16),
                          jax.ShapeDtypeStruct((128, HW), F32), jax.ShapeDtypeStruct((8, HW), F32)]
            out_specs += [cmap(0), cmap(0), cmap(0), cmap(0, 128), pl.BlockSpec((128, HW), fixed),
                          pl.BlockSpec((8, HW), fixed)]
    return pl.pallas_call(
        body, name=name, grid=(n_chunks,), out_shape=out_shape, in_specs=in_specs, out_specs=out_specs,
        scratch_shapes=[pltpu.VMEM((2, NH, HD, HD), F32)],
        compiler_params=_cparams(dimension_semantics=("arbitrary",)),
    )(*args)


def _scan_bwd(p, side, states, d_o, n_ctx_chunks, fw, branch, name):
    rows = p.shape[0]
    n_chunks = rows // CHUNK
    d_idx = 0 if fw else 1
    hg = branch == "hg"
    cols = (C_HQ, C_HI, C_HF_FW + d_idx) if hg else (C_GQ, C_GK, C_GV)

    def body(*refs):
        if hg:
            a_ref, b_ref, c_ref, lb_ref, st_ref, do_ref, da_ref, db_ref, dc_ref, dlb_ref, dstate = refs
        else:
            (a_ref, b_ref, c_ref, lr_ref, wgk_ref, bgk_ref, st_ref, do_ref, da_ref, db_ref, dc_ref, dlr_ref,
             dwgk_ref, dbias_ref, dstate) = refs
            lr = lr_ref[...]
            logits = _dot(lr, wgk_ref[...]) + bgk_ref[...]
            g_all = _log_sigmoid(logits) * (1.0 / GATE_NORM)

        @pl.when(pl.program_id(0) == 0)
        def _():
            dstate[...] = jnp.zeros_like(dstate)
            if hg:
                dlb_ref[...] = jnp.zeros_like(dlb_ref)
            else:
                dwgk_ref[...] = jnp.zeros_like(dwgk_ref)
                dbias_ref[...] = jnp.zeros_like(dbias_ref)

        dg_parts = []
        for h in range(NH):
            sl = slice(h * HD, (h + 1) * HD)
            if hg:
                hq, hf = a_ref[:, sl], c_ref[:, sl]
                q, k, g, f, sg, lb = _hg_inputs(hq, hf, lb_ref[...], d_idx, sl)
                v = b_ref[:, sl]
            else:
                q, k, v, g = a_ref[:, sl] * (HD ** -0.5), b_ref[:, sl], c_ref[:, sl], g_all[:, sl]
            dq, dk, dv, dg, dst0 = _chunk_bwd(q, k, v, g, st_ref[0, h], do_ref[:, sl], dstate[h], fw)
            dstate[h] = dst0
            if hg:
                da_ref[:, sl] = dq * _dsilu(hq)
                db_ref[:, sl] = dv
                df = dg / f - dk
                dc_ref[:, sl] = df * (1.0 - lb) * sg * (1.0 - sg)
                dlb_ref[0:1, sl] += _colsum(df * (1.0 - sg))
            else:
                da_ref[:, sl] = dq * (HD ** -0.5)
                db_ref[:, sl] = dk
                dc_ref[:, sl] = dv
                dg_parts.append(dg)
        if not hg:
            dlogits = jnp.concatenate(dg_parts, axis=1) * (1.0 / GATE_NORM) * (1.0 - _sigmoid(logits))
            dlr_ref[...] = _dot_nt(dlogits, wgk_ref[...])
            dwgk_ref[...] += _dot_tn(lr, dlogits)
            dbias_ref[0:1, :] += _colsum(dlogits)

    def chunk_of(j):
        return _chunk_index(n_chunks - 1 - j, n_ctx_chunks, n_chunks, fw)

    def cmap(blk, width=HW):
        return pl.BlockSpec((CHUNK, width), lambda j: (chunk_of(j), blk))

    fixed = lambda j: (0, 0)
    st_spec = pl.BlockSpec((1, NH, HD, HD), lambda j: (chunk_of(j), 0, 0, 0))
    big = jax.ShapeDtypeStruct((rows, HW), F32)
    if hg:
        in_specs = [cmap(cols[0]), cmap(cols[1]), cmap(cols[2]), pl.BlockSpec((4, HW), fixed), st_spec, cmap(0)]
        args = (p, p, p, side, states, d_o)
        out_shape = [big, big, big, jax.ShapeDtypeStruct((8, HW), F32)]
        out_specs = [cmap(0), cmap(0), cmap(0), pl.BlockSpec((8, HW), fixed)]
    else:
        in_specs = [cmap(cols[0]), cmap(cols[1]), cmap(cols[2]), cmap(OFF_LR // 128, 128),
                    pl.BlockSpec((128, HW), fixed), pl.BlockSpec((1, HW), fixed), st_spec, cmap(0)]
        args = (p, p, p, p, side[0], side[1], states, d_o)
        out_shape = [big, big, big, jax.ShapeDtypeStruct((rows, 128), F32), jax.ShapeDtypeStruct((128, HW), F32),
                     jax.ShapeDtypeStruct((8, HW), F32)]
        out_specs = [cmap(0), cmap(0), cmap(0), cmap(0, 128), pl.BlockSpec((128, HW), fixed),
                     pl.BlockSpec((8, HW), fixed)]
    return pl.pallas_call(
        body, name=name, grid=(n_chunks,), out_shape=out_shape, in_specs=in_specs, out_specs=out_specs,
        scratch_shapes=[pltpu.VMEM((NH, HD, HD), F32)],
        compiler_params=_cparams(dimension_semantics=("arbitrary",)),
    )(*args)


SMALL_ROWS = 56
ROWS_MOD_X = (0, 1, 8, 16, 17, 18)
ROWS_MOD_C = (2, 3)
ROW_PRE1, ROW_POST1, ROW_ONORM, ROW_PRE2, ROW_POST2, ROW_LB, ROW_BGK, ROW_WGK = 4, 9, 10, 19, 20, 24, 32, 40
ROW_LOSS = 21


def _reduce_small(gathered, lb_full, name):
    _, _, d = gathered.shape

    def body(g_ref, lb_ref, sum_ref, dmod_ref, dbmod_ref, dlb_ref):
        total = g_ref[0]
        for b in range(1, N_DEV):
            total = total + g_ref[b]
        sum_ref[...] = total
        dmod_ref[...] = jnp.zeros_like(dmod_ref)
        for m in range(N_MOD):
            col = slice(m * d, (m + 1) * d)
            acc = jnp.zeros((1, d), F32)
            for b in range(N_DEV):
                row = g_ref[b, ROWS_MOD_X[m]:ROWS_MOD_X[m] + 1, :]
                dmod_ref[b:b + 1, col] = row
                acc = acc + row
            if m < 2:
                ctx_row = total[ROWS_MOD_C[m]:ROWS_MOD_C[m] + 1, :]
                dmod_ref[8:9, col] = ctx_row
                acc = acc + ctx_row
            dbmod_ref[:, col] = acc
        lbv = lb_ref[...]
        for dd in range(2):
            lb = _sigmoid(lbv[dd:dd + 1, :] - lbv[2 + dd:3 + dd, :])
            gl = total[ROW_LB:ROW_LB + 1, dd * HW:(dd + 1) * HW] * lb * (1.0 - lb)
            dlb_ref[dd:dd + 1, :] = gl
            dlb_ref[2 + dd:3 + dd, :] = -gl

    return pl.pallas_call(
        body, name=name,
        out_shape=[jax.ShapeDtypeStruct((SMALL_ROWS, d), F32), jax.ShapeDtypeStruct((16, N_MOD * d), F32),
                   jax.ShapeDtypeStruct((1, N_MOD * d), F32), jax.ShapeDtypeStruct((4, HW), F32)],
        in_specs=[VMEM_SPEC] * 2, out_specs=[VMEM_SPEC] * 4, compiler_params=_cparams(),
    )(gathered, lb_full)


def _c_ctx_grad(gathered, c_ctx_row, name):
    def body(g_ref, c_ref, o_ref):
        acc = g_ref[0, 0:1, :]
        for chip in range(1, N_CHIP):
            acc = acc + g_ref[2 * chip, 0:1, :]
        o_ref[...] = acc * _dsilu(c_ref[...])

    return pl.pallas_call(
        body, name=name, out_shape=jax.ShapeDtypeStruct(c_ctx_row.shape, F32),
        in_specs=[VMEM_SPEC] * 2, out_specs=VMEM_SPEC, compiler_params=_cparams(),
    )(gathered, c_ctx_row)


def _relayout_w_in(w):
    pad = jnp.zeros((w.shape[0], 128 - 2 * RANK), w.dtype)
    return jnp.concatenate([w[:, :9 * HW], w[:, 9 * HW + 2 * RANK:], w[:, 9 * HW:9 * HW + 2 * RANK], pad], axis=1)


def _relayout_w_in_rows(wt):
    pad = jnp.zeros((128 - 2 * RANK, wt.shape[1]), wt.dtype)
    return jnp.concatenate([wt[:9 * HW], wt[9 * HW + 2 * RANK:], wt[9 * HW:9 * HW + 2 * RANK], pad], axis=0)


def _w_in_grad_rows(g_main, g_lr):
    return jnp.concatenate([g_main[:9 * HW], g_lr[:2 * RANK], g_main[9 * HW:]], axis=0)


def _w_in_grad_blocks(g_main, g_lr, n_blocks):
    lr0 = 9 * HW
    n = (g_main.shape[1] + 2 * RANK) // n_blocks

    def cols(lo, hi):
        out = []
        if lo < lr0:
            out.append(g_main[:, lo:min(hi, lr0)])
        if hi > lr0 and lo < lr0 + 2 * RANK:
            out.append(g_lr[:, max(lo, lr0) - lr0:min(hi, lr0 + 2 * RANK) - lr0])
        if hi > lr0 + 2 * RANK:
            out.append(g_main[:, max(lo, lr0 + 2 * RANK) - 2 * RANK:hi - 2 * RANK])
        return out

    return jnp.stack([jnp.concatenate(cols(j * n, (j + 1) * n), axis=1) for j in range(n_blocks)])


def _blocked(full, n_blocks):
    k, n = full.shape
    return full.reshape(k, n_blocks, n // n_blocks).transpose(1, 0, 2)


def _unblocked(blocks):
    nb, k, n = blocks.shape
    return blocks.transpose(1, 0, 2).reshape(k, nb * n)


def _sample_front(x0, ctx0, modc, modx, norm_pre1, lb_full, gla_side, w_in_r):
    ctx_len = ctx0.shape[0]
    n_ctx_tiles = ctx_len // TM
    n_ctx_chunks = ctx_len // CHUNK
    h1, p = _in_projection(ctx0, x0, modc, modx, norm_pre1, w_in_r, n_ctx_tiles, "in_projection")
    o_hg_fw, st_hg_fw, o_hg_bw, st_hg_bw = _scan_fwd_both(p, lb_full, n_ctx_chunks, "hg", "scan_hg")
    o_gla_fw, st_gla_fw, o_gla_bw, st_gla_bw = _scan_fwd_both(p, gla_side, n_ctx_chunks, "gla", "scan_gla")
    return dict(h1=h1, p=p, o_list=[o_hg_fw, o_hg_bw, o_gla_fw, o_gla_bw],
                states=[st_hg_fw, st_hg_bw, st_gla_fw, st_gla_bw])


def _sample_back(reduce, front, x0, ctx0, target0, modc, modx, norm_pre1, norms, onorms, lb_full, gla_side, w_in_r,
                 wbh, wbg, wout, ffn_weights):
    seq, d = x0.shape
    ctx_len = ctx0.shape[0]
    n_ctx_tiles = ctx_len // TM
    n_tiles = (ctx_len + seq) // TM
    n_ctx_chunks = ctx_len // CHUNK
    h1, p, o_list = front["h1"], front["p"], front["o_list"]
    st_hg_fw, st_hg_bw, st_gla_fw, st_gla_bw = front["states"]
    z2, y1, merged, og_hg, og_gla = _mixer_tail_fwd(x0, p, o_list, modx, norms, onorms, wbh, wbg, wout, n_ctx_tiles,
                                                    "mixer_tail")
    wg, wu, wd = ffn_weights([z2])
    loss_part, dz2, h2, a_act, du, dv, dy2, stat_ffn = _ffn_fwd_bwd(z2, modx, norms, wg, wu, wd, target0, "ffn")
    dff = wg.shape[0]
    tok = reduce("ffn", [_weight_grad(du, h2, "grad_w_ff_gate", tk=dff // 2, tn=d),
                         _weight_grad(dv, h2, "grad_w_ff_up", tk=dff // 2, tn=d),
                         _weight_grad(a_act, dy2, "grad_w_ff_down", tk=dff // 2)])

    (d_ohg, d_ogla, d_hgate, d_ggate, d_ghg, d_ggla, dy1, db_hg, db_gla, stat_mix) = _mixer_tail_bwd(
        x0, p, o_list, dz2, y1, modx + tok, norms, onorms, wbh, wbg, wout, n_ctx_tiles, n_tiles, "mixer_tail_bwd")
    tok = reduce("mix", [_weight_grad(og_hg, db_hg, "grad_w_br_hg"), _weight_grad(og_gla, db_gla, "grad_w_br_gla"),
                         _weight_grad(merged, dy1, "grad_w_out")])
    tok = tok + reduce("push_ffn", [dy1])
    gla_b = [(wgk, bias + tok) for wgk, bias in gla_side]
    (dgq_f, dgk_f, dgv_f, dlr_f, dwgk_f, dbgk_f, dgq_b, dgk_b, dgv_b, dlr_b, dwgk_b, dbgk_b) = _scan_bwd_both(
        p, gla_b, (st_gla_fw, st_gla_bw), d_ogla, n_ctx_chunks, "gla", "scan_gla_bwd")
    lb_b = lb_full + reduce("push_mix", [dbgk_f])
    (dhq_f, dhi_f, dhf_f, dlb_f, dhq_b, dhi_b, dhf_b, dlb_b) = _scan_bwd_both(
        p, lb_b, (st_hg_fw, st_hg_bw), d_ohg, n_ctx_chunks, "hg", "scan_hg_bwd")
    pieces = [dhq_f, dhq_b, dhi_f, dhi_b, dhf_f, dhf_b, d_hgate, dgq_f, dgq_b, dgk_f, dgk_b, dgv_f, dgv_b, d_ggate,
              d_ghg, d_ggla, dlr_f, dlr_b]
    dp, grad_x, stat_in = _in_projection_bwd(ctx0, x0, dz2, modc, modx, norm_pre1, w_in_r, pieces, n_ctx_tiles,
                                             "in_projection_bwd")

    tok = reduce("in", [_w_in_grad(dp, h1, w_in_r.shape[0], "grad_w_in")])
    reduce("small", dict(stat_in=stat_in + tok, stat_mix=stat_mix, stat_ffn=stat_ffn, dlb=(dlb_f, dlb_b),
                         dwgk=(dwgk_f, dwgk_b), dbgk=(dbgk_f, dbgk_b)))
    reduce("push_in", [])
    return dict(
        loss_part=loss_part, grad_x=grad_x, stat_in=stat_in, stat_mix=stat_mix, stat_ffn=stat_ffn,
        dlb=(dlb_f, dlb_b), dwgk=(dwgk_f, dwgk_b), dbgk=(dbgk_f, dbgk_b))


def kernel(x, c, ctx, c_ctx, w_mod, b_mod, norm_pre1, norm_post1, norm_pre2, norm_post2, w_in, hg_lb, hg_onorm, gla_w_gk, gla_b_gk, gla_onorm, w_br_hg, w_br_gla, w_out, w_ff_gate, w_ff_up, w_ff_down, loss_target, m_c_ctx, m_w_mod, m_b_mod, m_norm_pre1, m_norm_post1, m_norm_pre2, m_norm_post2, m_w_in, m_hg_lb, m_hg_onorm, m_gla_w_gk, m_gla_b_gk, m_gla_onorm, m_w_br_hg, m_w_br_gla, m_w_out, m_w_ff_gate, m_w_ff_up, m_w_ff_down, v_c_ctx, v_w_mod, v_b_mod, v_norm_pre1, v_norm_post1, v_norm_pre2, v_norm_post2, v_w_in, v_hg_lb, v_hg_onorm, v_gla_w_gk, v_gla_b_gk, v_gla_onorm, v_w_br_hg, v_w_br_gla, v_w_out, v_w_ff_gate, v_w_ff_up, v_w_ff_down):
    seq, d = x.shape[1], x.shape[2]
    ctx_len = ctx.shape[1]
    assert seq % TM == 0 and ctx_len % TM == 0 and d == 2 * HW
    ax, ay, ac = lax.axis_index("x"), lax.axis_index("y"), lax.axis_index("c")
    chip = 2 * ax + ay
    dev = 2 * chip + ac
    c_arr = jnp.reshape(ac, (1,)).astype(jnp.int32)
    chip_arr = jnp.reshape(chip, (1,)).astype(jnp.int32)
    transposed = ("w_in", "w_ff_gate", "w_ff_up")
    view = lambda a, nm: a[0].T if nm in transposed else a[0]

    sems_in, lands_in, token_in0 = _blocks_start([_cast_into_blocks(chip_arr, view(w_in, "w_in"), "cast_w_in")],
                                                 "gather_w_in_start")

    nc = d // 128
    pad8 = lambda a: jnp.pad(a, ((0, -a.shape[0] % 8), (0, 0)))
    small1 = jnp.concatenate([c.reshape(nc, 128) + token_in0[0, 0], pad8(hg_lb.reshape(4, 128)),
                              gla_w_gk.reshape(2 * RANK, 128), pad8(gla_b_gk.reshape(2, 128))], axis=0)
    blocks = [_cast_into_blocks(chip_arr, view(w_, nm), "cast_" + nm) for w_, nm in (
        (w_br_hg, "w_br_hg"), (w_br_gla, "w_br_gla"), (w_out, "w_out"), (w_ff_gate, "w_ff_gate"),
        (w_ff_up, "w_ff_up"), (w_ff_down, "w_ff_down"))]
    got1 = _allgather8(small1, "gather_small_params", after=blocks)
    c_all = got1[:, :nc, :].reshape(N_DEV, d)
    per_chip = got1[0::2]
    lb_full = per_chip[:, nc:nc + 4, :].transpose(1, 0, 2).reshape(4, HW)
    wgk_full = per_chip[:, nc + 8:nc + 8 + 2 * RANK, :].transpose(1, 0, 2).reshape(2, RANK, HW)
    bgk_full = per_chip[:, nc + 8 + 2 * RANK:nc + 10 + 2 * RANK, :].transpose(1, 0, 2).reshape(2, HW)
    wgk_pad = [jnp.zeros((128, HW), F32).at[dd * RANK:(dd + 1) * RANK].set(wgk_full[dd]) for dd in range(2)]
    bgk = [bgk_full[dd:dd + 1] for dd in range(2)]

    n_mod_cols = w_mod.shape[2]
    cond = jnp.concatenate([c_all, pad8(c_ctx.reshape(1, d))], axis=0)
    b_cols = lax.dynamic_slice(b_mod, (0, chip * n_mod_cols), (1, n_mod_cols))
    mod_part = _mod_forward(cond, w_mod[0], b_cols, "mod_forward")
    mod_got = _allgather8(mod_part, "gather_mod")
    mod_all = mod_got[0::2].transpose(1, 0, 2).reshape(16, N_CHIP * n_mod_cols)
    modx = pad8(lax.dynamic_slice(mod_all, (dev, 0), (1, N_MOD * d)).reshape(N_MOD, d))
    modc = pad8(mod_all[8].reshape(N_MOD, d))

    lands_in = _blocks_wait(sems_in, lands_in, [mod_got], "gather_w_in_wait")
    gathered_in = _blocks_finish(lands_in, "gather_w_in_finish")
    sems, lands, token = _blocks_start(blocks, "gather_rest_start", after=[gathered_in[0]])
    w_in_r = gathered_in[0].reshape(-1, d)

    norms = jnp.concatenate([norm_pre1, norm_post1, norm_pre2, norm_post2, jnp.zeros((4, d), F32)], axis=0)
    onorms = jnp.zeros((8, d), F32).at[0, :HD].set(hg_onorm[0]).at[1, :HD].set(gla_onorm[0])
    gla_side = [(wgk_pad[dd], bgk[dd]) for dd in range(2)]
    modx = modx + token[0, 0]
    front = _sample_front(x[0], ctx[0], modc, modx, norm_pre1, lb_full, gla_side, w_in_r)
    lands = _blocks_wait(sems, lands, front["o_list"], "gather_rest_wait")
    gathered = _blocks_finish(lands[:3], "gather_mix_finish")
    wbh, wbg = _unblocked(gathered[0]), _unblocked(gathered[1])
    wout = gathered[2].reshape(d, d)
    ffn_sems, ffn_lands, ffn_token = _forward_start(lands[3:], "gather_ffn_forward_start")
    onorms = onorms + ffn_token[0, 0]

    def ffn_weights(after):
        got = _forward_wait(ffn_sems, ffn_lands, after, "gather_ffn_forward_wait")
        return tuple(g.reshape(-1, d) for g in got)

    dff = w_ff_down.shape[1] * N_CHIP
    groups = {"ffn": ["w_ff_gate", "w_ff_up", "w_ff_down"], "mix": ["w_br_hg", "w_br_gla", "w_out"], "in": ["w_in"]}
    row_sharded = {"w_out": d // N_CHIP, "w_ff_down": dff // N_CHIP, "w_ff_gate": dff // N_CHIP,
                   "w_ff_up": dff // N_CHIP, "w_in": w_in.shape[2]}
    in_flight, to_sibling = {}, {}

    small = {}

    def reduce_small(stats):
        small2 = jnp.concatenate([
            stats["stat_in"], stats["stat_mix"], stats["stat_ffn"],
            jnp.concatenate(stats["dlb"], axis=1), jnp.concatenate(stats["dbgk"], axis=1),
            jnp.concatenate([stats["dwgk"][0][0:RANK], stats["dwgk"][1][RANK:2 * RANK]], axis=1)], axis=0)
        assert small2.shape[0] == SMALL_ROWS
        got2 = _allgather8(small2, "gather_small_grads")
        total, dmod_all, g_b_mod, g_lb_full = _reduce_small(got2, lb_full, "reduce_small")
        dmod_cols = lax.dynamic_slice(dmod_all, (0, chip * n_mod_cols), (16, n_mod_cols))
        g_w_mod, cctx_part = _mod_backward(cond, w_mod[0], dmod_cols, "mod_backward")
        got3 = _allgather8(cctx_part, "gather_c_ctx_grad")
        g_c_ctx = _c_ctx_grad(got3, c_ctx.reshape(1, d), "c_ctx_grad")
        small.update(total=total, g_b_mod=g_b_mod, g_lb_full=g_lb_full, g_w_mod=g_w_mod, g_c_ctx=g_c_ctx)

    def reduce(group, grads):
        if group == "small":
            return reduce_small(grads)
        if group.startswith("push_"):
            return push(group[5:], grads)
        nms = groups[group]
        full = [g.reshape(N_CHIP, row_sharded[nm], d) if nm in row_sharded else _blocked(g, N_CHIP)
                for g, nm in zip(grads, nms)]
        sems_, full, lands_, token_ = _send_half_start(full, "grads_to_sibling_start_" + group)
        to_sibling[group] = (sems_, full, lands_)
        return token_[0, 0]

    def push(group, after):
        nms = groups[group]
        sems_, full, lands_ = to_sibling[group]
        if group == "in":
            after = list(after) + [small["g_c_ctx"], small["total"]]
        full, from_sibling = _send_half_wait(sems_, full, lands_, after, "grads_to_sibling_wait_" + group)
        pairs = [_pair_sum(c_arr, f, r_, "pair_sum_" + nm) for f, r_, nm in zip(full, from_sibling, nms)]
        after = [small["g_c_ctx"], small["total"]] if group == "in" else []
        sems_, pairs, lands_, token_ = _scatter_start(pairs, "grads_to_owner_start_" + group, after)
        in_flight[group] = (sems_, pairs, lands_, token_)
        return token_[0, 0]

    r = _sample_back(reduce, front, x[0], ctx[0], loss_target[0], modc, modx, norm_pre1, norms, onorms, lb_full,
                     gla_side, w_in_r, wbh, wbg, wout, ffn_weights)
    loss_part, grad_x, stat_in, stat_mix, stat_ffn = (r[k] for k in ("loss_part", "grad_x", "stat_in", "stat_mix",
                                                                     "stat_ffn"))
    (dlb_f, dlb_b), (dwgk_f, dwgk_b), (dbgk_f, dbgk_b) = r["dlb"], r["dwgk"], r["dbgk"]

    weights = dict(w_in=(w_in, m_w_in, v_w_in), w_br_hg=(w_br_hg, m_w_br_hg, v_w_br_hg),
                   w_br_gla=(w_br_gla, m_w_br_gla, v_w_br_gla), w_out=(w_out, m_w_out, v_w_out),
                   w_ff_gate=(w_ff_gate, m_w_ff_gate, v_w_ff_gate), w_ff_up=(w_ff_up, m_w_ff_up, v_w_ff_up),
                   w_ff_down=(w_ff_down, m_w_ff_down, v_w_ff_down))
    names = ["w_in", "w_br_hg", "w_br_gla", "w_out", "w_ff_gate", "w_ff_up", "w_ff_down"]
    big = {}

    swapping = {}

    def sum_and_swap(group, after):
        sems_, pairs, lands_, _ = in_flight[group]
        pairs, lands_ = _scatter_wait(sems_, pairs, lands_, after, "grads_to_owner_wait_" + group)
        own_half = [_sum_owner(chip_arr, pr, g, "chip_sum_" + nm) for pr, g, nm in zip(pairs, lands_, groups[group])]
        swapping[group] = _swap_start(own_half, "halves_to_sibling_start_" + group)
        return own_half[-1]

    def update(group, after):
        sems_, own_half, lands_ = swapping[group]
        own_half, other_half = _swap_wait(sems_, own_half, lands_, after, "halves_to_sibling_wait_" + group)
        done = []
        for nm, own, oth in zip(groups[group], own_half, other_half):
            w_, m_, v_ = (view(a, nm) for a in weights[nm])
            res = _adamw_halves(c_arr, own, oth, w_, m_, v_, "adamw_" + nm)
            big[nm] = [r_.T[None] if nm in transposed else r_[None] for r_ in res]
            done.append(res[1])
        return done

    token_in = in_flight["in"][3]
    summed_ffn = sum_and_swap("ffn", [token_in])
    summed_mix = sum_and_swap("mix", [summed_ffn])

    total, g_b_mod, g_lb_full, g_w_mod, g_c_ctx = (small[k] for k in ("total", "g_b_mod", "g_lb_full", "g_w_mod",
                                                                      "g_c_ctx"))
    g_pre1, g_post1, g_pre2, g_post2 = (total[r_:r_ + 1] for r_ in (ROW_PRE1, ROW_POST1, ROW_PRE2, ROW_POST2))
    g_hg_on, g_gla_on = total[ROW_ONORM:ROW_ONORM + 1, 0:HD], total[ROW_ONORM:ROW_ONORM + 1, HD:2 * HD]
    n_lb = hg_lb.shape[2]
    g_hg_lb = lax.dynamic_slice(g_lb_full, (0, chip * n_lb), (4, n_lb))
    g_bgk = lax.dynamic_slice(total[ROW_BGK:ROW_BGK + 1].reshape(2, HW), (0, chip * n_lb), (2, n_lb))
    g_wgk_full = total[ROW_WGK:ROW_WGK + RANK].reshape(RANK, 2, HW).transpose(1, 0, 2).reshape(2 * RANK, HW)
    g_wgk = lax.dynamic_slice(g_wgk_full, (0, chip * n_lb), (2 * RANK, n_lb))

    small_items = [
        (g_c_ctx, c_ctx.reshape(1, d), m_c_ctx.reshape(1, d), v_c_ctx.reshape(1, d)),
        (g_b_mod, b_mod, m_b_mod, v_b_mod),
        (g_pre1, norm_pre1, m_norm_pre1, v_norm_pre1),
        (g_post1, norm_post1, m_norm_post1, v_norm_post1),
        (g_pre2, norm_pre2, m_norm_pre2, v_norm_pre2),
        (g_post2, norm_post2, m_norm_post2, v_norm_post2),
        (g_hg_lb, hg_lb.reshape(4, n_lb), m_hg_lb.reshape(4, n_lb), v_hg_lb.reshape(4, n_lb)),
        (g_hg_on, hg_onorm, m_hg_onorm, v_hg_onorm),
        (g_wgk, gla_w_gk.reshape(2 * RANK, n_lb), m_gla_w_gk.reshape(2 * RANK, n_lb), v_gla_w_gk.reshape(2 * RANK, n_lb)),
        (g_bgk, gla_b_gk.reshape(2, n_lb), m_gla_b_gk.reshape(2, n_lb), v_gla_b_gk.reshape(2, n_lb)),
        (g_gla_on, gla_onorm, m_gla_onorm, v_gla_onorm),
    ]
    small_res = _adamw_whole(small_items, "adamw_small")
    mod_res = _adamw_tiled(g_w_mod, w_mod[0], m_w_mod[0], v_w_mod[0], "adamw_w_mod")
    done_ffn = update("ffn", [summed_mix, mod_res[0], small_res[0][0]])
    done_mix = update("mix", done_ffn)
    update("in", [sum_and_swap("in", done_mix)])

    loss = total[ROW_LOSS, 0]

    shapes = dict(c_ctx=c_ctx.shape, b_mod=b_mod.shape, norm_pre1=norm_pre1.shape, norm_post1=norm_post1.shape,
                  norm_pre2=norm_pre2.shape, norm_post2=norm_post2.shape, hg_lb=hg_lb.shape, hg_onorm=hg_onorm.shape,
                  gla_w_gk=gla_w_gk.shape, gla_b_gk=gla_b_gk.shape, gla_onorm=gla_onorm.shape)
    small_names = ["c_ctx", "b_mod", "norm_pre1", "norm_post1", "norm_pre2", "norm_post2", "hg_lb", "hg_onorm",
                   "gla_w_gk", "gla_b_gk", "gla_onorm"]
    grads, deltas, new_m, new_v = {}, {}, {}, {}
    for nm, item, res in zip(small_names, small_items, small_res):
        grads[nm] = item[0].reshape(shapes[nm])
        deltas[nm], new_m[nm], new_v[nm] = (r.reshape(shapes[nm]) for r in res)
    grads["w_mod"] = g_w_mod[None]
    deltas["w_mod"], new_m["w_mod"], new_v["w_mod"] = (r[None] for r in mod_res)
    for nm in names:
        grads[nm], deltas[nm], new_m[nm], new_v[nm] = big[nm]
    order = ["c_ctx", "w_mod", "b_mod", "norm_pre1", "norm_post1", "norm_pre2", "norm_post2", "w_in", "hg_lb",
             "hg_onorm", "gla_w_gk", "gla_b_gk", "gla_onorm", "w_br_hg", "w_br_gla", "w_out", "w_ff_gate", "w_ff_up",
             "w_ff_down"]
    return (loss, grad_x[None], *[grads[n] for n in order], *[deltas[n] for n in order],
            *[new_m[n] for n in order], *[new_v[n] for n in order])


def _weight_grad_cols(xs, dy, n_cols, name, tn=512):
    rows = dy.shape[0]
    k = tk = xs.shape[1]

    return pl.pallas_call(
        functools.partial(_transposed_lhs_matmul), name=name, grid=(k // tk, n_cols // tn),
        out_shape=jax.ShapeDtypeStruct((k, n_cols), F32),
        in_specs=[pl.BlockSpec((rows, tk), lambda i, j: (0, i)), pl.BlockSpec((rows, tn), lambda i, j: (0, j))],
        out_specs=pl.BlockSpec((tk, tn), lambda i, j: (i, j)),
        scratch_shapes=[pltpu.VMEM((tk, rows), BF16)],
        compiler_params=_cparams(dimension_semantics=("parallel", "arbitrary")),
    )(xs, dy)
```

```python
import functools

import jax
import jax.numpy as jnp
from jax import lax
from jax.experimental import pallas as pl
from jax.experimental.pallas import tpu as pltpu

F32 = jnp.float32
BF16 = jnp.bfloat16
HIGHEST = lax.Precision.HIGHEST
MESH = pl.DeviceIdType.MESH

EPS = 1e-6
CHUNK = 64
SUB = 16
NSUB = CHUNK // SUB
NH = 4
HD = 128
HW = NH * HD
RANK = 16
GATE_NORM = 16.0
N_MOD = 6
TM = 256
TM_FFN = 128
N_DEV = 8
N_CHIP = 4
VMEM_LIMIT = 56 * 1024 * 1024

ADAM_LR = 0.001
ADAM_B1 = 0.9
ADAM_B2 = 0.999
ADAM_EPS = 1e-08
ADAM_WD = 0.01
ADAM_STEP = 10

VMEM_SPEC = pl.BlockSpec(memory_space=pltpu.VMEM)
ANY_SPEC = pl.BlockSpec(memory_space=pl.ANY)
HBM_SPEC = pl.BlockSpec(memory_space=pltpu.HBM)
SEM_SPEC = pl.BlockSpec(memory_space=pltpu.SEMAPHORE)
EFFECT = pltpu.SideEffectType.DATAFLOW_SIDE_EFFECTING


def _cparams(**kw):
    return pltpu.CompilerParams(vmem_limit_bytes=VMEM_LIMIT, **kw)


def _dot(a, b):
    return jnp.dot(a.astype(BF16), b.astype(BF16), preferred_element_type=F32)


def _dot_nt(a, b):
    return lax.dot_general(a.astype(BF16), b.astype(BF16), (((1,), (1,)), ((), ())), preferred_element_type=F32)


def _dot_tn(a, b):
    return lax.dot_general(a.astype(BF16), b.astype(BF16), (((0,), (0,)), ((), ())), preferred_element_type=F32)


def _sigmoid(x):
    return 1.0 / (1.0 + jnp.exp(-x))


def _silu(x):
    return x * _sigmoid(x)


def _dsilu(x):
    s = _sigmoid(x)
    return s * (1.0 + x * (1.0 - s))


def _log_sigmoid(x):
    return jnp.minimum(x, 0.0) - jnp.log(1.0 + jnp.exp(-jnp.abs(x)))


def _colsum(a):
    return jnp.sum(a, axis=0, keepdims=True)


def _rms(a):
    r = lax.rsqrt(jnp.mean(a * a, axis=-1, keepdims=True) + EPS)
    return a * r, r


def _rms_bwd(dn, n, r):
    return r * (dn - n * jnp.mean(dn * n, axis=-1, keepdims=True))


def _place():
    x, y, c = lax.axis_index("x"), lax.axis_index("y"), lax.axis_index("c")
    chips = [(1 - x, y), (x, 1 - y), (1 - x, 1 - y)]
    return x, y, c, chips


def _allgather8(v, name, after=()):
    rows, cols = v.shape
    n_after = len(after)

    def body(x_ref, *rest):
        out_ref, send_sems, recv_sems, local_sem = rest[n_after:]
        x, y, c, chips = _place()
        me, sibling = (x, y, c), (x, y, 1 - c)

        def blk(px, py, pc):
            return out_ref.at[4 * px + 2 * py + pc]

        def copy(k, block, to, src=None):
            return pltpu.make_async_remote_copy(
                src_ref=blk(*block) if src is None else src, dst_ref=blk(*block),
                send_sem=send_sems.at[k], recv_sem=recv_sems.at[k], device_id=to, device_id_type=MESH)

        mine = pltpu.make_async_copy(x_ref, blk(*me), local_sem)
        mine.start()
        first = [copy(0, me, sibling, src=x_ref)]
        first += [copy(1 + j, me, (*chip, c), src=x_ref) for j, chip in enumerate(chips)]
        for cp in first:
            cp.start()
        passed = [copy(4 + j, (*chip, c), sibling) for j, chip in enumerate(chips)]
        for j, chip in enumerate(chips):
            copy(1 + j, (*chip, c), me).wait_recv()
            passed[j].start()
        copy(0, sibling, me).wait_recv()
        for j, chip in enumerate(chips):
            copy(4 + j, (*chip, 1 - c), me).wait_recv()
        for cp in first + passed:
            cp.wait_send()
        mine.wait()

    return pl.pallas_call(
        body, name=name,
        out_shape=jax.ShapeDtypeStruct((N_DEV, rows, cols), v.dtype),
        in_specs=[VMEM_SPEC] + [ANY_SPEC] * n_after, out_specs=VMEM_SPEC,
        scratch_shapes=[pltpu.SemaphoreType.DMA((7,)), pltpu.SemaphoreType.DMA((7,)), pltpu.SemaphoreType.DMA],
    )(v, *after)


def _cast_into_blocks(chip_arr, w, name):
    rows, cols = w.shape
    tr = _row_tile(rows, 16, 256)

    def body(chip_ref, w_ref, o_ref):
        o_ref[0] = w_ref[...].astype(BF16)

    return pl.pallas_call(
        body, name=name,
        grid_spec=pltpu.PrefetchScalarGridSpec(
            num_scalar_prefetch=1, grid=(rows // tr,),
            in_specs=[pl.BlockSpec((tr, cols), lambda i, chip_ref: (i, 0))],
            out_specs=pl.BlockSpec((1, tr, cols), lambda i, chip_ref: (chip_ref[0], i, 0))),
        out_shape=jax.ShapeDtypeStruct((N_CHIP, rows, cols), BF16),
        compiler_params=_cparams(dimension_semantics=("parallel",)),
    )(chip_arr, w)


def _halved_by_rows(shape):
    return (shape[1] // 2) % 16 == 0


def _half_of(ref, pc, block=None):
    lead = slice(None) if block is None else block
    if _halved_by_rows(ref.shape):
        h = ref.shape[1] // 2
        return ref.at[lead, pl.ds(pl.multiple_of(pc * h, 16), h), :]
    h = ref.shape[2] // 2
    return ref.at[lead, :, pl.ds(pl.multiple_of(pc * h, 128), h)]


def _half_shape(shape):
    return (shape[0], shape[1] // 2, shape[2]) if _halved_by_rows(shape) else (shape[0], shape[1], shape[2] // 2)


def _half_rows(ref, chip_id, pc):
    return _half_of(ref, pc, chip_id)


def _gather_blocks(lands, name, after=()):
    n = len(lands)
    n_in = n + len(after)

    def body(*refs):
        outs = refs[n_in:n_in + n]
        send_sems, recv_sems = refs[n_in + n:]
        x, y, c, chips = _place()
        me_chip = 2 * x + y
        sibling = (x, y, 1 - c)

        def copy(k, j, chip_id, pc, to):
            return pltpu.make_async_remote_copy(
                src_ref=_half_rows(outs[k], chip_id, pc), dst_ref=_half_rows(outs[k], chip_id, pc),
                send_sem=send_sems.at[k, j], recv_sem=recv_sems.at[k, j], device_id=to, device_id_type=MESH)

        started = []
        for k in range(n):
            for j, chip in enumerate(chips):
                cp = copy(k, j, me_chip, c, (*chip, c))
                cp.start()
                started.append(cp)
        for k in range(n):
            for j, (px, py) in enumerate(chips):
                copy(k, j, 2 * px + py, c, sibling).wait_recv()
                cp = copy(k, 3 + j, 2 * px + py, c, sibling)
                cp.start()
                started.append(cp)
        for k in range(n):
            for j, (px, py) in enumerate(chips):
                copy(k, 3 + j, 2 * px + py, 1 - c, sibling).wait_recv()
        for cp in started:
            cp.wait_send()

    return pl.pallas_call(
        body, name=name,
        out_shape=[jax.ShapeDtypeStruct(l.shape, l.dtype) for l in lands],
        in_specs=[ANY_SPEC] * n_in, out_specs=[ANY_SPEC] * n,
        input_output_aliases={i: i for i in range(n)},
        scratch_shapes=[pltpu.SemaphoreType.DMA((n, 6)), pltpu.SemaphoreType.DMA((n, 6))],
    )(*lands, *after)


def _hbm(a):
    return pltpu.with_memory_space_constraint(a, pltpu.HBM)


def _blocks_start(lands, name, after=()):
    n = len(lands)
    n_sem = 3 * n
    first = n + len(after)

    def body(*refs):
        lnd = refs[:n]
        send_sems, recv_sems = refs[first:first + n_sem], refs[first + n_sem:first + 2 * n_sem]
        token = refs[-1]
        x, y, c, chips = _place()
        me_chip = 2 * x + y
        for k in range(n):
            for j, chip in enumerate(chips):
                pltpu.make_async_remote_copy(
                    src_ref=_half_rows(lnd[k], me_chip, c), dst_ref=_half_rows(lnd[k], me_chip, c),
                    send_sem=send_sems[3 * k + j], recv_sem=recv_sems[3 * k + j],
                    device_id=(*chip, c), device_id_type=MESH).start()
        token[...] = jnp.zeros_like(token)

    out = pl.pallas_call(
        body, name=name,
        out_shape=(*[pltpu.SemaphoreType.DMA(())] * (2 * n_sem),
                   *[pltpu.HBM(l.shape, l.dtype) for l in lands],
                   jax.ShapeDtypeStruct((8, 128), F32)),
        in_specs=[HBM_SPEC] * n + [ANY_SPEC] * len(after),
        out_specs=(*[SEM_SPEC] * (2 * n_sem), *[HBM_SPEC] * n, VMEM_SPEC),
        input_output_aliases={i: 2 * n_sem + i for i in range(n)},
        compiler_params=pltpu.CompilerParams(has_side_effects=EFFECT),
    )(*[_hbm(l) for l in lands], *after)
    return list(out[:2 * n_sem]), list(out[2 * n_sem:2 * n_sem + n]), out[-1]


def _blocks_wait(sems, lands, after, name):
    n = len(lands)
    n_sem = 3 * n

    def body(*refs):
        lnd = refs[:n]
        s_sems, r_sems = refs[n:n + n_sem], refs[n + n_sem:n + 2 * n_sem]
        x, y, c, chips = _place()
        me_chip = 2 * x + y
        for k in range(n):
            for j, (px, py) in enumerate(chips):
                cp = pltpu.make_async_remote_copy(
                    src_ref=_half_rows(lnd[k], me_chip, c), dst_ref=_half_rows(lnd[k], 2 * px + py, c),
                    send_sem=s_sems[3 * k + j], recv_sem=r_sems[3 * k + j],
                    device_id=(px, py, c), device_id_type=MESH)
                cp.wait_send()
                cp.wait_recv()

    out = pl.pallas_call(
        body, name=name,
        out_shape=tuple(pltpu.HBM(l.shape, l.dtype) for l in lands),
        in_specs=[HBM_SPEC] * n + [SEM_SPEC] * (2 * n_sem) + [ANY_SPEC] * len(after),
        out_specs=[HBM_SPEC] * n,
        input_output_aliases={i: i for i in range(n)},
        compiler_params=pltpu.CompilerParams(has_side_effects=EFFECT),
    )(*lands, *sems, *after)
    return list(out)


def _forward_start(lands, name):
    n = len(lands)
    n_sem = 3 * n

    def body(*refs):
        lnd = refs[:n]
        send_sems, recv_sems = refs[n:n + n_sem], refs[n + n_sem:n + 2 * n_sem]
        x, y, c, chips = _place()
        for k in range(n):
            for j, (px, py) in enumerate(chips):
                pltpu.make_async_remote_copy(
                    src_ref=_half_rows(lnd[k], 2 * px + py, c), dst_ref=_half_rows(lnd[k], 2 * px + py, c),
                    send_sem=send_sems[3 * k + j], recv_sem=recv_sems[3 * k + j],
                    device_id=(x, y, 1 - c), device_id_type=MESH).start()
        refs[-1][...] = jnp.zeros_like(refs[-1])

    out = pl.pallas_call(
        body, name=name,
        out_shape=(*[pltpu.SemaphoreType.DMA(())] * (2 * n_sem), *[pltpu.HBM(l.shape, l.dtype) for l in lands],
                   jax.ShapeDtypeStruct((8, 128), F32)),
        in_specs=[HBM_SPEC] * n,
        out_specs=(*[SEM_SPEC] * (2 * n_sem), *[HBM_SPEC] * n, VMEM_SPEC),
        input_output_aliases={i: 2 * n_sem + i for i in range(n)},
        compiler_params=pltpu.CompilerParams(has_side_effects=EFFECT),
    )(*[_hbm(l) for l in lands])
    return list(out[:2 * n_sem]), list(out[2 * n_sem:2 * n_sem + n]), out[-1]


def _forward_wait(sems, lands, after, name):
    n = len(lands)
    n_sem = 3 * n

    def body(*refs):
        lnd = refs[:n]
        s_sems, r_sems = refs[n:n + n_sem], refs[n + n_sem:n + 2 * n_sem]
        x, y, c, chips = _place()
        for k in range(n):
            for j, (px, py) in enumerate(chips):
                cp = pltpu.make_async_remote_copy(
                    src_ref=_half_rows(lnd[k], 2 * px + py, c), dst_ref=_half_rows(lnd[k], 2 * px + py, 1 - c),
                    send_sem=s_sems[3 * k + j], recv_sem=r_sems[3 * k + j],
                    device_id=(x, y, 1 - c), device_id_type=MESH)
                cp.wait_send()
                cp.wait_recv()

    out = pl.pallas_call(
        body, name=name,
        out_shape=tuple(pltpu.HBM(l.shape, l.dtype) for l in lands),
        in_specs=[HBM_SPEC] * n + [SEM_SPEC] * (2 * n_sem) + [ANY_SPEC] * len(after),
        out_specs=[HBM_SPEC] * n,
        input_output_aliases={i: i for i in range(n)},
        compiler_params=pltpu.CompilerParams(has_side_effects=EFFECT),
    )(*lands, *sems, *after)
    return list(out)


def _blocks_finish(lands, name):
    n = len(lands)

    def body(*refs):
        lnd = refs[n:2 * n]
        send_sems, recv_sems = refs[2 * n:]
        x, y, c, chips = _place()
        sibling = (x, y, 1 - c)

        def copy(k, j, chip_id, pc):
            return pltpu.make_async_remote_copy(
                src_ref=_half_rows(lnd[k], chip_id, pc), dst_ref=_half_rows(lnd[k], chip_id, pc),
                send_sem=send_sems.at[k, j], recv_sem=recv_sems.at[k, j], device_id=sibling, device_id_type=MESH)

        started = []
        for k in range(n):
            for j, (px, py) in enumerate(chips):
                cp = copy(k, j, 2 * px + py, c)
                cp.start()
                started.append(cp)
        for k in range(n):
            for j, (px, py) in enumerate(chips):
                copy(k, j, 2 * px + py, 1 - c).wait_recv()
        for cp in started:
            cp.wait_send()

    out = pl.pallas_call(
        body, name=name,
        out_shape=[jax.ShapeDtypeStruct(l.shape, l.dtype) for l in lands],
        in_specs=[ANY_SPEC] * n, out_specs=[ANY_SPEC] * n,
        input_output_aliases={i: i for i in range(n)},
        scratch_shapes=[pltpu.SemaphoreType.DMA((n, 3)), pltpu.SemaphoreType.DMA((n, 3))],
    )(*lands)
    return list(out)


def _gather_start(shards, name):
    n = len(shards)
    n_sem = 3 * n

    def body(*refs):
        ins, lands = refs[:n], refs[n:2 * n]
        send_sems, recv_sems = refs[2 * n:2 * n + n_sem], refs[2 * n + n_sem:2 * n + 2 * n_sem]
        token = refs[-1]
        x, y, c, chips = _place()
        me_chip = 2 * x + y
        for k in range(n):
            h = shards[k].shape[0] // 2
            rows = pl.ds(pl.multiple_of(c * h, 8), h)
            for j, chip in enumerate(chips):
                pltpu.make_async_remote_copy(
                    src_ref=ins[k].at[rows, :], dst_ref=lands[k].at[me_chip, rows, :],
                    send_sem=send_sems[3 * k + j], recv_sem=recv_sems[3 * k + j],
                    device_id=(*chip, c), device_id_type=MESH).start()
        token[...] = jnp.zeros_like(token)

    lands = [_hbm(lax.empty((N_CHIP,) + s.shape, s.dtype)) for s in shards]
    out = pl.pallas_call(
        body, name=name,
        out_shape=(*[pltpu.SemaphoreType.DMA(())] * (2 * n_sem),
                   *[pltpu.HBM(s.shape, s.dtype) for s in shards],
                   *[pltpu.HBM(l.shape, l.dtype) for l in lands],
                   jax.ShapeDtypeStruct((8, 128), F32)),
        in_specs=[HBM_SPEC] * (2 * n),
        out_specs=(*[SEM_SPEC] * (2 * n_sem), *[HBM_SPEC] * (2 * n), VMEM_SPEC),
        input_output_aliases={i: 2 * n_sem + i for i in range(2 * n)},
        compiler_params=pltpu.CompilerParams(has_side_effects=EFFECT),
    )(*[_hbm(s) for s in shards], *lands)
    sems = list(out[:2 * n_sem])
    return sems, list(out[2 * n_sem:2 * n_sem + n]), list(out[2 * n_sem + n:2 * n_sem + 2 * n]), out[-1]


def _gather_wait(sems, shards, lands, after, name):
    n = len(shards)
    n_sem = 3 * n

    def body(*refs):
        ins, lnd = refs[:n], refs[n:2 * n]
        s_sems, r_sems = refs[2 * n:2 * n + n_sem], refs[2 * n + n_sem:2 * n + 2 * n_sem]
        x, y, c, chips = _place()
        for k in range(n):
            h = shards[k].shape[0] // 2
            rows = pl.ds(pl.multiple_of(c * h, 8), h)
            for j, (px, py) in enumerate(chips):
                cp = pltpu.make_async_remote_copy(
                    src_ref=ins[k].at[rows, :], dst_ref=lnd[k].at[2 * px + py, rows, :],
                    send_sem=s_sems[3 * k + j], recv_sem=r_sems[3 * k + j],
                    device_id=(px, py, c), device_id_type=MESH)
                cp.wait_send()
                cp.wait_recv()

    out = pl.pallas_call(
        body, name=name,
        out_shape=(*[pltpu.HBM(s.shape, s.dtype) for s in shards], *[pltpu.HBM(l.shape, l.dtype) for l in lands]),
        in_specs=[HBM_SPEC] * (2 * n) + [SEM_SPEC] * (2 * n_sem) + [ANY_SPEC],
        out_specs=[HBM_SPEC] * (2 * n),
        input_output_aliases={i: i for i in range(2 * n)},
        compiler_params=pltpu.CompilerParams(has_side_effects=EFFECT),
    )(*shards, *lands, *sems, after)
    return list(out[:n]), list(out[n:])


def _gather_finish(shards, lands, name):
    n = len(shards)

    def body(*refs):
        ins, lnd = refs[:n], refs[2 * n:3 * n]
        send_sems, recv_sems, local_sems = refs[3 * n:]
        x, y, c, chips = _place()
        me_chip = 2 * x + y
        sibling = (x, y, 1 - c)

        def half(k, chip_id, pc):
            h = shards[k].shape[0] // 2
            return lnd[k].at[chip_id, pl.ds(pl.multiple_of(pc * h, 8), h), :]

        def copy(k, j, chip_id, pc):
            return pltpu.make_async_remote_copy(
                src_ref=half(k, chip_id, pc), dst_ref=half(k, chip_id, pc),
                send_sem=send_sems.at[k, j], recv_sem=recv_sems.at[k, j], device_id=sibling, device_id_type=MESH)

        locals_, started = [], []
        for k in range(n):
            cp = pltpu.make_async_copy(ins[k], lnd[k].at[me_chip], local_sems.at[k])
            cp.start()
            locals_.append(cp)
            for j, (px, py) in enumerate(chips):
                cp = copy(k, j, 2 * px + py, c)
                cp.start()
                started.append(cp)
        for k in range(n):
            for j, (px, py) in enumerate(chips):
                copy(k, j, 2 * px + py, 1 - c).wait_recv()
        for cp in started:
            cp.wait_send()
        for cp in locals_:
            cp.wait()

    out = pl.pallas_call(
        body, name=name,
        out_shape=[jax.ShapeDtypeStruct(l.shape, l.dtype) for l in lands],
        in_specs=[ANY_SPEC] * (2 * n), out_specs=[ANY_SPEC] * n,
        input_output_aliases={n + i: i for i in range(n)},
        scratch_shapes=[pltpu.SemaphoreType.DMA((n, 3)), pltpu.SemaphoreType.DMA((n, 3)),
                        pltpu.SemaphoreType.DMA((n,))],
    )(*shards, *lands)
    return list(out)


def _send_other_half(arrs, name):
    n = len(arrs)

    def body(*refs):
        ins, outs = refs[:n], refs[n:2 * n]
        send_sems, recv_sems = refs[2 * n:]
        x, y, c, _ = _place()
        cps = []
        for k in range(n):
            cp = pltpu.make_async_remote_copy(
                src_ref=_half_of(ins[k], 1 - c), dst_ref=outs[k],
                send_sem=send_sems.at[k], recv_sem=recv_sems.at[k], device_id=(x, y, 1 - c), device_id_type=MESH)
            cp.start()
            cps.append(cp)
        for cp in cps:
            cp.wait()

    return pl.pallas_call(
        body, name=name,
        out_shape=[jax.ShapeDtypeStruct(_half_shape(a.shape), a.dtype) for a in arrs],
        in_specs=[ANY_SPEC] * n, out_specs=[ANY_SPEC] * n,
        scratch_shapes=[pltpu.SemaphoreType.DMA((n,)), pltpu.SemaphoreType.DMA((n,))],
    )(*arrs)


def _send_half_start(arrs, name):
    n = len(arrs)

    def body(*refs):
        ins, lnd = refs[:n], refs[n:2 * n]
        send_sems, recv_sems = refs[2 * n:3 * n], refs[3 * n:4 * n]
        token = refs[-1]
        x, y, c, _ = _place()
        for k in range(n):
            pltpu.make_async_remote_copy(
                src_ref=_half_of(ins[k], 1 - c), dst_ref=lnd[k], send_sem=send_sems[k], recv_sem=recv_sems[k],
                device_id=(x, y, 1 - c), device_id_type=MESH).start()
        token[...] = jnp.zeros_like(token)

    lands = [_hbm(lax.empty(_half_shape(a.shape), a.dtype)) for a in arrs]
    out = pl.pallas_call(
        body, name=name,
        out_shape=(*[pltpu.SemaphoreType.DMA(())] * (2 * n), *[pltpu.HBM(a.shape, a.dtype) for a in arrs],
                   *[pltpu.HBM(l.shape, l.dtype) for l in lands], jax.ShapeDtypeStruct((8, 128), F32)),
        in_specs=[HBM_SPEC] * (2 * n),
        out_specs=(*[SEM_SPEC] * (2 * n), *[HBM_SPEC] * (2 * n), VMEM_SPEC),
        input_output_aliases={i: 2 * n + i for i in range(2 * n)},
        compiler_params=pltpu.CompilerParams(has_side_effects=EFFECT),
    )(*[_hbm(a) for a in arrs], *lands)
    return list(out[:2 * n]), list(out[2 * n:3 * n]), list(out[3 * n:4 * n]), out[-1]


def _send_half_wait(sems, arrs, lands, after, name):
    n = len(arrs)

    def body(*refs):
        ins, lnd = refs[:n], refs[n:2 * n]
        s_sems, r_sems = refs[2 * n:3 * n], refs[3 * n:4 * n]
        x, y, c, _ = _place()
        for k in range(n):
            cp = pltpu.make_async_remote_copy(
                src_ref=_half_of(ins[k], 1 - c), dst_ref=lnd[k], send_sem=s_sems[k], recv_sem=r_sems[k],
                device_id=(x, y, 1 - c), device_id_type=MESH)
            cp.wait_send()
            cp.wait_recv()

    out = pl.pallas_call(
        body, name=name,
        out_shape=tuple(pltpu.HBM(a.shape, a.dtype) for a in list(arrs) + list(lands)),
        in_specs=[HBM_SPEC] * (2 * n) + [SEM_SPEC] * (2 * n) + [ANY_SPEC] * len(after),
        out_specs=[HBM_SPEC] * (2 * n),
        input_output_aliases={i: i for i in range(2 * n)},
        compiler_params=pltpu.CompilerParams(has_side_effects=EFFECT),
    )(*arrs, *lands, *sems, *after)
    return list(out[:n]), list(out[n:])


def _blocks_to_owner(arrs, name):
    n = len(arrs)

    def body(*refs):
        ins, outs = refs[:n], refs[n:2 * n]
        send_sems, recv_sems, local_sems = refs[2 * n:]
        x, y, c, chips = _place()
        me_chip = 2 * x + y
        locals_, started = [], []
        for k in range(n):
            cp = pltpu.make_async_copy(ins[k].at[me_chip], outs[k].at[me_chip], local_sems.at[k])
            cp.start()
            locals_.append(cp)

        def copy(k, j, src_block, dst_slot, to):
            return pltpu.make_async_remote_copy(
                src_ref=ins[k].at[src_block], dst_ref=outs[k].at[dst_slot],
                send_sem=send_sems.at[k, j], recv_sem=recv_sems.at[k, j], device_id=to, device_id_type=MESH)

        for k in range(n):
            for j, (px, py) in enumerate(chips):
                cp = copy(k, j, 2 * px + py, me_chip, (px, py, c))
                cp.start()
                started.append(cp)
        for k in range(n):
            for j, (px, py) in enumerate(chips):
                copy(k, j, me_chip, 2 * px + py, (px, py, c)).wait_recv()
        for cp in started:
            cp.wait_send()
        for cp in locals_:
            cp.wait()

    return pl.pallas_call(
        body, name=name,
        out_shape=[jax.ShapeDtypeStruct(a.shape, a.dtype) for a in arrs],
        in_specs=[ANY_SPEC] * n, out_specs=[ANY_SPEC] * n,
        scratch_shapes=[pltpu.SemaphoreType.DMA((n, 3)), pltpu.SemaphoreType.DMA((n, 3)),
                        pltpu.SemaphoreType.DMA((n,))],
    )(*arrs)


def _scatter_blocks(arrs, name):
    n = len(arrs)

    def body(*refs):
        ins, outs = refs[:n], refs[n:2 * n]
        send_sems, recv_sems = refs[2 * n:]
        x, y, c, chips = _place()
        me_chip = 2 * x + y

        def copy(k, j, src_block, dst_slot, to):
            return pltpu.make_async_remote_copy(
                src_ref=ins[k].at[src_block], dst_ref=outs[k].at[dst_slot],
                send_sem=send_sems.at[k, j], recv_sem=recv_sems.at[k, j], device_id=to, device_id_type=MESH)

        started = []
        for k in range(n):
            for j, (px, py) in enumerate(chips):
                cp = copy(k, j, 2 * px + py, me_chip, (px, py, c))
                cp.start()
                started.append(cp)
        for k in range(n):
            for j, (px, py) in enumerate(chips):
                copy(k, j, me_chip, 2 * px + py, (px, py, c)).wait_recv()
        for cp in started:
            cp.wait_send()

    return pl.pallas_call(
        body, name=name,
        out_shape=[jax.ShapeDtypeStruct(a.shape, a.dtype) for a in arrs],
        in_specs=[ANY_SPEC] * n, out_specs=[ANY_SPEC] * n,
        scratch_shapes=[pltpu.SemaphoreType.DMA((n, 3)), pltpu.SemaphoreType.DMA((n, 3))],
    )(*arrs)


def _scatter_start(arrs, name, after=()):
    n = len(arrs)
    n_sem = 3 * n
    first = 2 * n + len(after)

    def body(*refs):
        ins, lnd = refs[:n], refs[n:2 * n]
        send_sems, recv_sems = refs[first:first + n_sem], refs[first + n_sem:first + 2 * n_sem]
        token = refs[-1]
        x, y, c, chips = _place()
        me_chip = 2 * x + y
        for k in range(n):
            for j, (px, py) in enumerate(chips):
                pltpu.make_async_remote_copy(
                    src_ref=ins[k].at[2 * px + py], dst_ref=lnd[k].at[me_chip],
                    send_sem=send_sems[3 * k + j], recv_sem=recv_sems[3 * k + j],
                    device_id=(px, py, c), device_id_type=MESH).start()
        token[...] = jnp.zeros_like(token)

    lands = [_hbm(lax.empty(a.shape, a.dtype)) for a in arrs]
    out = pl.pallas_call(
        body, name=name,
        out_shape=(*[pltpu.SemaphoreType.DMA(())] * (2 * n_sem),
                   *[pltpu.HBM(a.shape, a.dtype) for a in arrs], *[pltpu.HBM(a.shape, a.dtype) for a in arrs],
                   jax.ShapeDtypeStruct((8, 128), F32)),
        in_specs=[HBM_SPEC] * (2 * n) + [ANY_SPEC] * len(after),
        out_specs=(*[SEM_SPEC] * (2 * n_sem), *[HBM_SPEC] * (2 * n), VMEM_SPEC),
        input_output_aliases={i: 2 * n_sem + i for i in range(2 * n)},
        compiler_params=pltpu.CompilerParams(has_side_effects=EFFECT),
    )(*[_hbm(a) for a in arrs], *lands, *after)
    base = 2 * n_sem
    return list(out[:base]), list(out[base:base + n]), list(out[base + n:base + 2 * n]), out[-1]


def _scatter_wait(sems, arrs, lands, after, name):
    n = len(arrs)
    n_sem = 3 * n

    def body(*refs):
        ins, lnd = refs[:n], refs[n:2 * n]
        s_sems, r_sems = refs[2 * n:2 * n + n_sem], refs[2 * n + n_sem:2 * n + 2 * n_sem]
        x, y, c, chips = _place()
        for k in range(n):
            for j, (px, py) in enumerate(chips):
                cp = pltpu.make_async_remote_copy(
                    src_ref=ins[k].at[2 * px + py], dst_ref=lnd[k].at[2 * px + py],
                    send_sem=s_sems[3 * k + j], recv_sem=r_sems[3 * k + j],
                    device_id=(px, py, c), device_id_type=MESH)
                cp.wait_send()
                cp.wait_recv()

    out = pl.pallas_call(
        body, name=name,
        out_shape=tuple(pltpu.HBM(a.shape, a.dtype) for a in list(arrs) + list(lands)),
        in_specs=[HBM_SPEC] * (2 * n) + [SEM_SPEC] * (2 * n_sem) + [ANY_SPEC] * len(after),
        out_specs=[HBM_SPEC] * (2 * n),
        input_output_aliases={i: i for i in range(2 * n)},
        compiler_params=pltpu.CompilerParams(has_side_effects=EFFECT),
    )(*arrs, *lands, *sems, *after)
    return list(out[:n]), list(out[n:])


def _sum_owner(chip_arr, pairs, got, name):
    nb, h, cols = got.shape
    tr = _row_tile(h, 16, 256)

    def body(chip_ref, own_ref, a_ref, b_ref, c_ref, o_ref):
        o_ref[...] = ((own_ref[0].astype(F32) + a_ref[0].astype(F32)) + b_ref[0].astype(F32)) + c_ref[0].astype(F32)

    def slot(off):
        return pl.BlockSpec((1, tr, cols), lambda i, chip_ref: ((chip_ref[0] + off) % N_CHIP, i, 0))

    return pl.pallas_call(
        body, name=name,
        grid_spec=pltpu.PrefetchScalarGridSpec(
            num_scalar_prefetch=1, grid=(h // tr,),
            in_specs=[slot(0), slot(1), slot(2), slot(3)],
            out_specs=pl.BlockSpec((tr, cols), lambda i, chip_ref: (i, 0))),
        out_shape=jax.ShapeDtypeStruct((h, cols), F32),
        compiler_params=_cparams(dimension_semantics=("parallel",)),
    )(chip_arr, pairs, got, got, got)


def _swap_start(arrs, name, after=()):
    n = len(arrs)
    first = 2 * n + len(after)

    def body(*refs):
        ins, lnd = refs[:n], refs[n:2 * n]
        send_sems, recv_sems = refs[first:first + n], refs[first + n:first + 2 * n]
        x, y, c, _ = _place()
        for k in range(n):
            pltpu.make_async_remote_copy(
                src_ref=ins[k], dst_ref=lnd[k], send_sem=send_sems[k], recv_sem=recv_sems[k],
                device_id=(x, y, 1 - c), device_id_type=MESH).start()

    lands = [_hbm(lax.empty(a.shape, a.dtype)) for a in arrs]
    out = pl.pallas_call(
        body, name=name,
        out_shape=(*[pltpu.SemaphoreType.DMA(())] * (2 * n), *[pltpu.HBM(a.shape, a.dtype) for a in arrs],
                   *[pltpu.HBM(a.shape, a.dtype) for a in arrs]),
        in_specs=[HBM_SPEC] * (2 * n) + [ANY_SPEC] * len(after),
        out_specs=(*[SEM_SPEC] * (2 * n), *[HBM_SPEC] * (2 * n)),
        input_output_aliases={i: 2 * n + i for i in range(2 * n)},
        compiler_params=pltpu.CompilerParams(has_side_effects=EFFECT),
    )(*[_hbm(a) for a in arrs], *lands, *after)
    return list(out[:2 * n]), list(out[2 * n:3 * n]), list(out[3 * n:4 * n])


def _swap_wait(sems, arrs, lands, after, name):
    n = len(arrs)

    def body(*refs):
        ins, lnd = refs[:n], refs[n:2 * n]
        s_sems, r_sems = refs[2 * n:3 * n], refs[3 * n:4 * n]
        x, y, c, _ = _place()
        for k in range(n):
            cp = pltpu.make_async_remote_copy(
                src_ref=ins[k], dst_ref=lnd[k], send_sem=s_sems[k], recv_sem=r_sems[k],
                device_id=(x, y, 1 - c), device_id_type=MESH)
            cp.wait_send()
            cp.wait_recv()

    out = pl.pallas_call(
        body, name=name,
        out_shape=tuple(pltpu.HBM(a.shape, a.dtype) for a in list(arrs) + list(lands)),
        in_specs=[HBM_SPEC] * (2 * n) + [SEM_SPEC] * (2 * n) + [ANY_SPEC] * len(after),
        out_specs=[HBM_SPEC] * (2 * n),
        input_output_aliases={i: i for i in range(2 * n)},
        compiler_params=pltpu.CompilerParams(has_side_effects=EFFECT),
    )(*arrs, *lands, *sems, *after)
    return list(out[:n]), list(out[n:])


def _swap_with_sibling(arrs, name):
    n = len(arrs)

    def body(*refs):
        ins, outs = refs[:n], refs[n:2 * n]
        send_sems, recv_sems = refs[2 * n:]
        x, y, c, _ = _place()
        cps = []
        for k in range(n):
            cp = pltpu.make_async_remote_copy(
                src_ref=ins[k], dst_ref=outs[k], send_sem=send_sems.at[k], recv_sem=recv_sems.at[k],
                device_id=(x, y, 1 - c), device_id_type=MESH)
            cp.start()
            cps.append(cp)
        for cp in cps:
            cp.wait()

    return pl.pallas_call(
        body, name=name,
        out_shape=[jax.ShapeDtypeStruct(a.shape, a.dtype) for a in arrs],
        in_specs=[ANY_SPEC] * n, out_specs=[ANY_SPEC] * n,
        scratch_shapes=[pltpu.SemaphoreType.DMA((n,)), pltpu.SemaphoreType.DMA((n,))],
    )(*arrs)


def _row_tile(h, mult=8, cap=128):
    for t in range(cap - cap % mult, mult - 1, -mult):
        if h % t == 0:
            return t
    if mult > 8:
        return _row_tile(h, 8, cap)
    raise ValueError(h)


def _cast_bf16(a, name):
    rows, cols = a.shape
    tr = _row_tile(rows, 16, 256)

    def body(a_ref, o_ref):
        o_ref[...] = a_ref[...].astype(BF16)

    return pl.pallas_call(
        body, name=name, grid=(rows // tr,),
        out_shape=jax.ShapeDtypeStruct(a.shape, BF16),
        in_specs=[pl.BlockSpec((tr, cols), lambda i: (i, 0))],
        out_specs=pl.BlockSpec((tr, cols), lambda i: (i, 0)),
        compiler_params=_cparams(dimension_semantics=("parallel",)),
    )(a)


def _pair_sum(c_arr, full, recv, name):
    nb, rows, cols = full.shape

    def body(c_ref, f_ref, r_ref, o_ref):
        o_ref[...] = (f_ref[...] + r_ref[...]).astype(BF16)

    if _halved_by_rows(full.shape):
        h = rows // 2
        tr = _row_tile(h, 16, 256)
        steps = h // tr
        own = pl.BlockSpec((1, tr, cols), lambda b, i, c_ref: (b, c_ref[0] * steps + i, 0))
        half = pl.BlockSpec((1, tr, cols), lambda b, i, c_ref: (b, i, 0))
    else:
        steps = 1
        own = pl.BlockSpec((1, rows, cols // 2), lambda b, i, c_ref: (b, 0, c_ref[0]))
        half = pl.BlockSpec((1, rows, cols // 2), lambda b, i, c_ref: (b, 0, 0))
    return pl.pallas_call(
        body, name=name,
        grid_spec=pltpu.PrefetchScalarGridSpec(
            num_scalar_prefetch=1, grid=(nb, steps), in_specs=[own, half], out_specs=half),
        out_shape=jax.ShapeDtypeStruct(_half_shape(full.shape), BF16),
        compiler_params=_cparams(dimension_semantics=("parallel", "parallel")),
    )(c_arr, full, recv)


def _sum_chips(got, name):
    nb, h, cols = got.shape
    tr = _row_tile(h, 16, 256)

    def body(g_ref, o_ref):
        g = g_ref[...].astype(F32)
        o_ref[...] = ((g[0] + g[1]) + g[2]) + g[3]

    return pl.pallas_call(
        body, name=name, grid=(h // tr,),
        out_shape=jax.ShapeDtypeStruct((h, cols), F32),
        in_specs=[pl.BlockSpec((nb, tr, cols), lambda i: (0, i, 0))],
        out_specs=pl.BlockSpec((tr, cols), lambda i: (i, 0)),
        compiler_params=_cparams(dimension_semantics=("parallel",)),
    )(got)


def _adam_math(g, w, m, v):
    m1 = ADAM_B1 * m + (1.0 - ADAM_B1) * g
    v1 = ADAM_B2 * v + (1.0 - ADAM_B2) * (g * g)
    m_hat = m1 / (1.0 - ADAM_B1 ** ADAM_STEP)
    v_hat = v1 / (1.0 - ADAM_B2 ** ADAM_STEP)
    delta = -ADAM_LR * (m_hat / (jnp.sqrt(v_hat) + ADAM_EPS) + ADAM_WD * w)
    return delta, m1, v1


def _adamw_halves(c_arr, own, other, w, m, v, name):
    rows, cols = w.shape
    by_rows = own.shape[1] == cols

    def body(c_ref, own_ref, oth_ref, w_ref, m_ref, v_ref, g_out, d_out, m_out, v_out):
        if by_rows:
            g = jnp.where(pl.program_id(0) == c_ref[0], own_ref[...], oth_ref[...])
        else:
            own_, oth_ = own_ref[...], oth_ref[...]
            g = jnp.where(c_ref[0] == 0, jnp.concatenate([own_, oth_], axis=1), jnp.concatenate([oth_, own_], axis=1))
        d, m1, v1 = _adam_math(g, w_ref[...], m_ref[...], v_ref[...])
        g_out[...] = g
        d_out[...] = d
        m_out[...] = m1
        v_out[...] = v1

    if by_rows:
        h = rows // 2
        tr = _row_tile(h)
        steps = h // tr
        grid = (2, steps)
        half_spec = pl.BlockSpec((tr, cols), lambda p, i, c_ref: (i, 0))
        full_spec = pl.BlockSpec((tr, cols), lambda p, i, c_ref: (p * steps + i, 0))
    else:
        tr = _row_tile(rows)
        grid = (1, rows // tr)
        half_spec = pl.BlockSpec((tr, cols // 2), lambda p, i, c_ref: (i, 0))
        full_spec = pl.BlockSpec((tr, cols), lambda p, i, c_ref: (i, 0))
    return pl.pallas_call(
        body, name=name,
        grid_spec=pltpu.PrefetchScalarGridSpec(
            num_scalar_prefetch=1, grid=grid,
            in_specs=[half_spec, half_spec, full_spec, full_spec, full_spec],
            out_specs=[full_spec] * 4),
        out_shape=[jax.ShapeDtypeStruct(w.shape, F32)] * 4,
        compiler_params=_cparams(dimension_semantics=("parallel", "parallel")),
    )(c_arr, own, other, w, m, v)


def _adamw_whole(items, name):
    n = len(items)

    def body(*refs):
        ins, outs = refs[:4 * n], refs[4 * n:]
        for k in range(n):
            g, w, m, v = (r[...] for r in ins[4 * k:4 * k + 4])
            d, m1, v1 = _adam_math(g, w, m, v)
            outs[3 * k][...] = d
            outs[3 * k + 1][...] = m1
            outs[3 * k + 2][...] = v1

    flat = [a for it in items for a in it]
    shapes = [jax.ShapeDtypeStruct(it[1].shape, F32) for it in items for _ in range(3)]
    out = pl.pallas_call(
        body, name=name, out_shape=shapes,
        in_specs=[VMEM_SPEC] * (4 * n), out_specs=[VMEM_SPEC] * (3 * n),
        compiler_params=_cparams(),
    )(*flat)
    return [tuple(out[3 * k:3 * k + 3]) for k in range(n)]


def _adamw_tiled(g, w, m, v, name):
    rows, cols = w.shape
    tr = _row_tile(rows)

    def body(g_ref, w_ref, m_ref, v_ref, d_out, m_out, v_out):
        d, m1, v1 = _adam_math(g_ref[...], w_ref[...], m_ref[...], v_ref[...])
        d_out[...] = d
        m_out[...] = m1
        v_out[...] = v1

    spec = pl.BlockSpec((tr, cols), lambda i: (i, 0))
    return pl.pallas_call(
        body, name=name, grid=(rows // tr,),
        out_shape=[jax.ShapeDtypeStruct(w.shape, F32)] * 3,
        in_specs=[spec] * 4, out_specs=[spec] * 3,
        compiler_params=_cparams(dimension_semantics=("parallel",)),
    )(g, w, m, v)


def _mod_forward(cond, w_mod, b_mod_cols, name):
    def body(c_ref, w_ref, b_ref, o_ref):
        o_ref[...] = _dot(_silu(c_ref[...]), w_ref[...]) + b_ref[...]

    return pl.pallas_call(
        body, name=name, out_shape=jax.ShapeDtypeStruct((cond.shape[0], w_mod.shape[1]), F32),
        in_specs=[VMEM_SPEC] * 3, out_specs=VMEM_SPEC, compiler_params=_cparams(),
    )(cond, w_mod, b_mod_cols)


def _mod_backward(cond, w_mod, dmod_cols, name):
    def body(c_ref, w_ref, d_ref, gw_ref, gc_ref):
        s = _silu(c_ref[...])
        d = d_ref[...]
        gw_ref[...] = _dot_tn(s, d)
        gc_ref[...] = _dot_nt(d[8:16, :], w_ref[...])

    return pl.pallas_call(
        body, name=name,
        out_shape=[jax.ShapeDtypeStruct(w_mod.shape, F32), jax.ShapeDtypeStruct((8, w_mod.shape[0]), F32)],
        in_specs=[VMEM_SPEC] * 3, out_specs=[VMEM_SPEC] * 2, compiler_params=_cparams(),
    )(cond, w_mod, dmod_cols)


def _col_chunks(width, step=512):
    return [(s, min(step, width - s)) for s in range(0, width, step)]


def _w_in_row(p_off):
    if p_off < 9 * HW:
        return p_off
    return 9 * HW if p_off == OFF_LR else p_off + 2 * RANK


def _in_projection_cols(ctx0, x0, modc, modx, pre1, w_t, name):
    d = x0.shape[1]
    n_ctx, n_lat = ctx0.shape[0], x0.shape[0]
    rows = n_ctx + n_lat
    n_main = OFF_LR // HW

    def modulated(z, mod_ref, pre_ref):
        n, _ = _rms(z)
        return (n * pre_ref[...] * (1.0 + mod_ref[1:2, :]) + mod_ref[0:1, :]).astype(BF16)

    def body(ctx_ref, x_ref, modc_ref, modx_ref, pre_ref, w_ref, h_ref, p_ref):
        j = pl.program_id(0)

        @pl.when(j == 0)
        def _():
            h_ref[0:n_ctx, :] = modulated(ctx_ref[...], modc_ref, pre_ref)
            for r0 in range(0, n_lat, TM):
                h_ref[n_ctx + r0:n_ctx + r0 + TM, :] = modulated(x_ref[r0:r0 + TM, :], modx_ref, pre_ref)

        row = pl.multiple_of(jnp.where(j < 9, j * HW, j * HW + 2 * RANK), 32)
        p_ref[...] = _dot_nt(h_ref[...], w_ref[pl.ds(row, HW), :])

    fixed = lambda j: (0, 0)
    h1, p = pl.pallas_call(
        body, name=name, grid=(n_main,),
        out_shape=[jax.ShapeDtypeStruct((rows, d), BF16), jax.ShapeDtypeStruct((rows, P_WIDTH), F32)],
        in_specs=[VMEM_SPEC, VMEM_SPEC, pl.BlockSpec((8, d), fixed), pl.BlockSpec((8, d), fixed),
                  pl.BlockSpec((1, d), fixed), VMEM_SPEC],
        out_specs=[pl.BlockSpec((rows, d), fixed), pl.BlockSpec((rows, HW), lambda j: (0, j))],
        compiler_params=_cparams(dimension_semantics=("arbitrary",)),
    )(ctx0, x0, modc, modx, pre1, w_t)

    lr0 = _w_in_row(OFF_LR)

    def lr_body(h_ref, w_ref, p_in, p_ref):
        p_ref[...] = _dot_nt(h_ref[...], w_ref[...])

    p = pl.pallas_call(
        lr_body, name=name + "_lr", grid=(1,),
        out_shape=jax.ShapeDtypeStruct((rows, P_WIDTH), F32),
        in_specs=[pl.BlockSpec((rows, d), lambda j: (0, 0)), pl.BlockSpec((128, d), lambda j: (lr0 // 128, 0)),
                  ANY_SPEC],
        out_specs=pl.BlockSpec((rows, 128), lambda j: (0, OFF_LR // 128)),
        input_output_aliases={2: 0},
        compiler_params=_cparams(),
    )(h1, w_t, p)
    return h1, p


def _in_projection(ctx0, x0, modc, modx, pre1, w_t, n_ctx_tiles, name):
    d = x0.shape[1]
    rows = ctx0.shape[0] + x0.shape[0]
    width = P_WIDTH

    def body(ctx_ref, x_ref, modc_ref, modx_ref, pre_ref, w_ref, h_ref, p_ref):
        is_ctx = pl.program_id(0) < n_ctx_tiles
        n, _ = _rms(jnp.where(is_ctx, ctx_ref[...], x_ref[...]))
        shift = jnp.where(is_ctx, modc_ref[0:1, :], modx_ref[0:1, :])
        scale = jnp.where(is_ctx, modc_ref[1:2, :], modx_ref[1:2, :])
        h = (n * pre_ref[...] * (1.0 + scale) + shift).astype(BF16)
        h_ref[...] = h
        for s, w in _col_chunks(width):
            p_ref[:, s:s + w] = _dot_nt(h, w_ref[_w_in_row(s):_w_in_row(s) + w, :])

    row = lambda i: (i, 0)
    fixed = lambda i: (0, 0)
    return pl.pallas_call(
        body, name=name, grid=(rows // TM,),
        out_shape=[jax.ShapeDtypeStruct((rows, d), BF16), jax.ShapeDtypeStruct((rows, width), F32)],
        in_specs=[pl.BlockSpec((TM, d), lambda i: (jnp.minimum(i, n_ctx_tiles - 1), 0)),
                  pl.BlockSpec((TM, d), lambda i: (jnp.maximum(i - n_ctx_tiles, 0), 0)),
                  pl.BlockSpec((8, d), fixed), pl.BlockSpec((8, d), fixed), pl.BlockSpec((1, d), fixed), VMEM_SPEC],
        out_specs=[pl.BlockSpec((TM, d), row), pl.BlockSpec((TM, width), row)],
        compiler_params=_cparams(dimension_semantics=("parallel",)),
    )(ctx0, x0, modc, modx, pre1, w_t)


C_HQ, C_HI, C_HF_FW, C_HF_BW, C_HGATE, C_GQ, C_GK, C_GV, C_GGATE = range(9)
OFF_GATE_HG = 9 * HW
OFF_LR = 13 * HW
P_WIDTH = OFF_LR + 128


def _head_norm_fwd(o, w):
    outs, ns, rs = [], [], []
    for h in range(NH):
        n, r = _rms(o[:, h * HD:(h + 1) * HD])
        ns.append(n)
        rs.append(r)
        outs.append(n * w)
    return jnp.concatenate(outs, axis=1), ns, rs


def _mixer_tail(z, o_hg, o_gla, p_hgate, p_ggate, p_gate_hg, p_gate_gla, hg_on, gla_on, wbh, wbg, wout):
    on_hg, n_hg, r_hg = _head_norm_fwd(o_hg, hg_on)
    on_gla, n_gla, r_gla = _head_norm_fwd(o_gla, gla_on)
    og_hg = (on_hg * _silu(p_hgate)).astype(BF16)
    og_gla = (on_gla * _silu(p_ggate)).astype(BF16)
    b_hg = jnp.dot(og_hg, wbh, preferred_element_type=F32)
    b_gla = jnp.dot(og_gla, wbg, preferred_element_type=F32)
    s_hg = _sigmoid(p_gate_hg)
    s_gla = _sigmoid(p_gate_gla)
    merged = (s_hg * b_hg + s_gla * b_gla).astype(BF16)
    y1 = jnp.dot(merged, wout, preferred_element_type=F32)
    return dict(on_hg=on_hg, n_hg=n_hg, r_hg=r_hg, on_gla=on_gla, n_gla=n_gla, r_gla=r_gla, og_hg=og_hg,
                og_gla=og_gla, b_hg=b_hg, b_gla=b_gla, s_hg=s_hg, s_gla=s_gla, merged=merged, y1=y1)


def _mixer_ffn(x_lat, p, o_list, modx, norms, onorms, w_br_hg, w_br_gla, w_out, w_gate, w_up, w_down, target,
               n_ctx_tiles, name):
    rows, d = x_lat.shape
    dff = w_gate.shape[0]
    inv_d = 1.0 / d

    def body(x_ref, ofw_hg, obw_hg, ofw_gla, obw_gla, p_hgate, p_ggate, p_ghg_a, p_ghg_b, p_ggla_a, p_ggla_b,
             modx_ref, norm_ref, on_ref, wbh_ref, wbg_ref, wout_ref, wg_ref, wu_ref, wd_ref, t_ref,
             loss_ref, dz2_ref, y1_ref, mrg_ref, oghg_ref, oggla_ref, h2_ref, a_ref, du_ref, dv_ref, dy2_ref,
             stat_ref):
        i = pl.program_id(0)
        post1, pre2, post2 = norm_ref[1:2, :], norm_ref[2:3, :], norm_ref[3:4, :]
        gate1, shift2, scale2, gate2 = modx_ref[2:3, :], modx_ref[3:4, :], modx_ref[4:5, :], modx_ref[5:6, :]
        p_gate_hg = jnp.concatenate([p_ghg_a[...], p_ghg_b[...]], axis=1)
        p_gate_gla = jnp.concatenate([p_ggla_a[...], p_ggla_b[...]], axis=1)
        t = _mixer_tail(x_ref[...], ofw_hg[...] + obw_hg[...], ofw_gla[...] + obw_gla[...], p_hgate[...],
                        p_ggate[...], p_gate_hg, p_gate_gla, on_ref[0:1, 0:HD], on_ref[1:2, 0:HD],
                        wbh_ref[...], wbg_ref[...], wout_ref[...])
        y1_ref[...] = t["y1"]
        mrg_ref[...] = t["merged"]
        oghg_ref[...] = t["og_hg"]
        oggla_ref[...] = t["og_gla"]
        n1, _ = _rms(t["y1"])
        z2 = x_ref[...] + n1 * post1 * gate1
        n2, r2 = _rms(z2)
        nw2 = n2 * pre2
        h2 = (nw2 * (1.0 + scale2) + shift2).astype(BF16)
        h2_ref[...] = h2
        u = _dot_nt(h2, wg_ref[...])
        v = _dot_nt(h2, wu_ref[...])
        su = _silu(u)
        a = (su * v).astype(BF16)
        a_ref[...] = a
        y2 = jnp.dot(a, wd_ref[...], preferred_element_type=F32)
        n3, r3 = _rms(y2)
        z3 = z2 + n3 * post2 * gate2
        err = z3 - t_ref[...]
        part = 0.5 * inv_d * jnp.sum(err * err)
        dz3 = err * inv_d
        dgate2 = _colsum(dz3 * n3 * post2)
        tt = dz3 * gate2
        dpost2 = _colsum(tt * n3)
        dy2 = _rms_bwd(tt * post2, n3, r3).astype(BF16)
        dy2_ref[...] = dy2
        da = _dot_nt(dy2, wd_ref[...])
        du = (da * v * _dsilu(u)).astype(BF16)
        dv = (da * su).astype(BF16)
        du_ref[...] = du
        dv_ref[...] = dv
        dh2 = (jnp.dot(du, wg_ref[...], preferred_element_type=F32)
               + jnp.dot(dv, wu_ref[...], preferred_element_type=F32))
        dshift2 = _colsum(dh2)
        dscale2 = _colsum(dh2 * nw2)
        dnw2 = dh2 * (1.0 + scale2)
        dpre2 = _colsum(dnw2 * n2)
        dz2_ref[...] = dz3 + _rms_bwd(dnw2 * pre2, n2, r2)

        @pl.when(i == 0)
        def _():
            stat_ref[...] = jnp.zeros_like(stat_ref)
            loss_ref[...] = jnp.zeros_like(loss_ref)

        for r, val in enumerate((dshift2, dscale2, dgate2, dpre2, dpost2)):
            stat_ref[r:r + 1, :] += val
        loss_ref[...] += part
        stat_ref[5:6, 0:128] += part

    tm = TM_FFN
    ctx_tiles = n_ctx_tiles * (TM // tm)
    lat = lambda i: (i, 0)
    full = lambda i: (i + ctx_tiles, 0)
    fixed = lambda i: (0, 0)

    def pcol(blk):
        return pl.BlockSpec((tm, HW), lambda i: (i + ctx_tiles, blk))

    in_specs = ([pl.BlockSpec((tm, d), lat)] + [pl.BlockSpec((tm, HW), full)] * 4
                + [pcol(C_HGATE), pcol(C_GGATE), pcol(9), pcol(10), pcol(11), pcol(12)]
                + [pl.BlockSpec((8, d), fixed), pl.BlockSpec((8, d), fixed), pl.BlockSpec((8, d), fixed)]
                + [VMEM_SPEC] * 6 + [pl.BlockSpec((tm, d), lat)])
    bf = lambda w: jax.ShapeDtypeStruct((rows, w), BF16)
    out_shape = [jax.ShapeDtypeStruct((8, 128), F32), jax.ShapeDtypeStruct((rows, d), F32),
                 jax.ShapeDtypeStruct((rows, d), F32), bf(d), bf(HW), bf(HW), bf(d), bf(dff), bf(dff), bf(dff), bf(d),
                 jax.ShapeDtypeStruct((8, d), F32)]
    out_specs = [pl.BlockSpec((8, 128), fixed), pl.BlockSpec((tm, d), lat), pl.BlockSpec((tm, d), lat),
                 pl.BlockSpec((tm, d), lat), pl.BlockSpec((tm, HW), lat), pl.BlockSpec((tm, HW), lat),
                 pl.BlockSpec((tm, d), lat), pl.BlockSpec((tm, dff), lat), pl.BlockSpec((tm, dff), lat),
                 pl.BlockSpec((tm, dff), lat), pl.BlockSpec((tm, d), lat), pl.BlockSpec((8, d), fixed)]
    return pl.pallas_call(
        body, name=name, grid=(rows // tm,), out_shape=out_shape, in_specs=in_specs, out_specs=out_specs,
        compiler_params=_cparams(dimension_semantics=("arbitrary",)),
    )(x_lat, *o_list, p, p, p, p, p, p, modx, norms, onorms, w_br_hg, w_br_gla, w_out, w_gate, w_up, w_down, target)


def _mixer_tail_fwd(x_lat, p, o_list, modx, norms, onorms, w_br_hg, w_br_gla, w_out, n_ctx_tiles, name):
    rows, d = x_lat.shape

    def body(x_ref, ofw_hg, obw_hg, ofw_gla, obw_gla, p_hgate, p_ggate, p_ghg_a, p_ghg_b, p_ggla_a, p_ggla_b,
             modx_ref, norm_ref, on_ref, wbh_ref, wbg_ref, wout_ref, z2_ref, y1_ref, mrg_ref, oghg_ref, oggla_ref):
        p_gate_hg = jnp.concatenate([p_ghg_a[...], p_ghg_b[...]], axis=1)
        p_gate_gla = jnp.concatenate([p_ggla_a[...], p_ggla_b[...]], axis=1)
        t = _mixer_tail(x_ref[...], ofw_hg[...] + obw_hg[...], ofw_gla[...] + obw_gla[...], p_hgate[...],
                        p_ggate[...], p_gate_hg, p_gate_gla, on_ref[0:1, 0:HD], on_ref[1:2, 0:HD],
                        wbh_ref[...], wbg_ref[...], wout_ref[...])
        y1_ref[...] = t["y1"]
        mrg_ref[...] = t["merged"]
        oghg_ref[...] = t["og_hg"]
        oggla_ref[...] = t["og_gla"]
        n1, _ = _rms(t["y1"])
        z2_ref[...] = x_ref[...] + n1 * norm_ref[1:2, :] * modx_ref[2:3, :]

    lat = lambda i: (i, 0)
    full = lambda i: (i + n_ctx_tiles, 0)
    fixed = lambda i: (0, 0)

    def pcol(blk):
        return pl.BlockSpec((TM, HW), lambda i: (i + n_ctx_tiles, blk))

    in_specs = ([pl.BlockSpec((TM, d), lat)] + [pl.BlockSpec((TM, HW), full)] * 4
                + [pcol(C_HGATE), pcol(C_GGATE), pcol(9), pcol(10), pcol(11), pcol(12)]
                + [pl.BlockSpec((8, d), fixed)] * 3 + [VMEM_SPEC] * 3)
    bf = lambda w: jax.ShapeDtypeStruct((rows, w), BF16)
    f32 = jax.ShapeDtypeStruct((rows, d), F32)
    return pl.pallas_call(
        body, name=name, grid=(rows // TM,), out_shape=[f32, f32, bf(d), bf(HW), bf(HW)], in_specs=in_specs,
        out_specs=[pl.BlockSpec((TM, d), lat)] * 3 + [pl.BlockSpec((TM, HW), lat)] * 2,
        compiler_params=_cparams(dimension_semantics=("parallel",)),
    )(x_lat, *o_list, p, p, p, p, p, p, modx, norms, onorms, w_br_hg, w_br_gla, w_out)


def _ffn_fwd_bwd(z2, modx, norms, w_gate, w_up, w_down, target, name):
    rows, d = z2.shape
    dff = w_gate.shape[0]
    inv_d = 1.0 / d

    def body(z2_ref, modx_ref, norm_ref, wg_ref, wu_ref, wd_ref, t_ref,
             loss_ref, dz2_ref, h2_ref, a_ref, du_ref, dv_ref, dy2_ref, stat_ref):
        i = pl.program_id(0)
        pre2, post2 = norm_ref[2:3, :], norm_ref[3:4, :]
        shift2, scale2, gate2 = modx_ref[3:4, :], modx_ref[4:5, :], modx_ref[5:6, :]
        z2 = z2_ref[...]
        n2, r2 = _rms(z2)
        nw2 = n2 * pre2
        h2 = (nw2 * (1.0 + scale2) + shift2).astype(BF16)
        h2_ref[...] = h2
        u = _dot_nt(h2, wg_ref[...])
        v = _dot_nt(h2, wu_ref[...])
        su = _silu(u)
        a = (su * v).astype(BF16)
        a_ref[...] = a
        y2 = jnp.dot(a, wd_ref[...], preferred_element_type=F32)
        n3, r3 = _rms(y2)
        err = z2 + n3 * post2 * gate2 - t_ref[...]
        part = 0.5 * inv_d * jnp.sum(err * err)
        dz3 = err * inv_d
        dgate2 = _colsum(dz3 * n3 * post2)
        tt = dz3 * gate2
        dpost2 = _colsum(tt * n3)
        dy2 = _rms_bwd(tt * post2, n3, r3).astype(BF16)
        dy2_ref[...] = dy2
        da = _dot_nt(dy2, wd_ref[...])
        du = (da * v * _dsilu(u)).astype(BF16)
        dv = (da * su).astype(BF16)
        du_ref[...] = du
        dv_ref[...] = dv
        dh2 = (jnp.dot(du, wg_ref[...], preferred_element_type=F32)
               + jnp.dot(dv, wu_ref[...], preferred_element_type=F32))
        dshift2 = _colsum(dh2)
        dscale2 = _colsum(dh2 * nw2)
        dnw2 = dh2 * (1.0 + scale2)
        dpre2 = _colsum(dnw2 * n2)
        dz2_ref[...] = dz3 + _rms_bwd(dnw2 * pre2, n2, r2)

        @pl.when(i == 0)
        def _():
            stat_ref[...] = jnp.zeros_like(stat_ref)
            loss_ref[...] = jnp.zeros_like(loss_ref)

        for r, val in enumerate((dshift2, dscale2, dgate2, dpre2, dpost2)):
            stat_ref[r:r + 1, :] += val
        loss_ref[...] += part
        stat_ref[5:6, 0:128] += part

    lat = lambda i: (i, 0)
    fixed = lambda i: (0, 0)
    bf = lambda w: jax.ShapeDtypeStruct((rows, w), BF16)
    return pl.pallas_call(
        body, name=name, grid=(rows // TM,),
        out_shape=[jax.ShapeDtypeStruct((8, 128), F32), jax.ShapeDtypeStruct((rows, d), F32), bf(d), bf(dff), bf(dff),
                   bf(dff), bf(d), jax.ShapeDtypeStruct((8, d), F32)],
        in_specs=[pl.BlockSpec((TM, d), lat), pl.BlockSpec((8, d), fixed), pl.BlockSpec((8, d), fixed)]
        + [VMEM_SPEC] * 3 + [pl.BlockSpec((TM, d), lat)],
        out_specs=[pl.BlockSpec((8, 128), fixed), pl.BlockSpec((TM, d), lat), pl.BlockSpec((TM, d), lat),
                   pl.BlockSpec((TM, dff), lat), pl.BlockSpec((TM, dff), lat), pl.BlockSpec((TM, dff), lat),
                   pl.BlockSpec((TM, d), lat), pl.BlockSpec((8, d), fixed)],
        compiler_params=_cparams(dimension_semantics=("arbitrary",)),
    )(z2, modx, norms, w_gate, w_up, w_down, target)


def _mixer_tail_bwd(x_lat, p, o_list, dz2, y1, modx, norms, onorms, w_br_hg, w_br_gla, w_out, n_ctx_tiles, n_tiles,
                    name):
    rows, d = x_lat.shape
    total = n_tiles * TM

    def body(x_ref, ofw_hg, obw_hg, ofw_gla, obw_gla, p_hgate, p_ggate, p_ghg_a, p_ghg_b, p_ggla_a, p_ggla_b,
             dz2_ref, y1_ref, modx_ref, norm_ref, on_ref, wbh_ref, wbg_ref, wout_ref,
             dohg_ref, dogla_ref, dhgate_ref, dggate_ref, dghg_ref, dggla_ref, dy1_ref, dbhg_ref, dbgla_ref,
             stat_ref):
        i = pl.program_id(0)

        @pl.when(i == 0)
        def _():
            stat_ref[...] = jnp.zeros_like(stat_ref)

        @pl.when(i < n_ctx_tiles)
        def _():
            for ref in (dohg_ref, dogla_ref, dhgate_ref, dggate_ref, dghg_ref, dggla_ref):
                ref[...] = jnp.zeros_like(ref)

        @pl.when(i >= n_ctx_tiles)
        def _():
            post1, gate1 = norm_ref[1:2, :], modx_ref[2:3, :]
            hg_on, gla_on = on_ref[0:1, 0:HD], on_ref[1:2, 0:HD]
            p_gate_hg = jnp.concatenate([p_ghg_a[...], p_ghg_b[...]], axis=1)
            p_gate_gla = jnp.concatenate([p_ggla_a[...], p_ggla_b[...]], axis=1)
            ph, pg = p_hgate[...], p_ggate[...]
            t = _mixer_tail(x_ref[...], ofw_hg[...] + obw_hg[...], ofw_gla[...] + obw_gla[...], ph, pg,
                            p_gate_hg, p_gate_gla, hg_on, gla_on, wbh_ref[...], wbg_ref[...], wout_ref[...])
            dz2 = dz2_ref[...]
            n1, r1 = _rms(y1_ref[...])
            dgate1 = _colsum(dz2 * n1 * post1)
            tt = dz2 * gate1
            dpost1 = _colsum(tt * n1)
            dy1 = _rms_bwd(tt * post1, n1, r1).astype(BF16)
            dy1_ref[...] = dy1
            dmerged = _dot_nt(dy1, wout_ref[...])
            dghg_ref[...] = (dmerged * t["b_hg"] * t["s_hg"] * (1.0 - t["s_hg"])).astype(BF16)
            dggla_ref[...] = (dmerged * t["b_gla"] * t["s_gla"] * (1.0 - t["s_gla"])).astype(BF16)
            db_hg = (dmerged * t["s_hg"]).astype(BF16)
            db_gla = (dmerged * t["s_gla"]).astype(BF16)
            dbhg_ref[...] = db_hg
            dbgla_ref[...] = db_gla
            don_acc = []
            for (db, wb, pgate, on, ns, rs, gain, gate_ref, do_ref) in (
                    (db_hg, wbh_ref, ph, t["on_hg"], t["n_hg"], t["r_hg"], hg_on, dhgate_ref, dohg_ref),
                    (db_gla, wbg_ref, pg, t["on_gla"], t["n_gla"], t["r_gla"], gla_on, dggate_ref, dogla_ref)):
                dog = _dot_nt(db, wb[...])
                gate_ref[...] = (dog * on * _dsilu(pgate)).astype(BF16)
                don = dog * _silu(pgate)
                acc = jnp.zeros((1, HD), F32)
                for h in range(NH):
                    sl = slice(h * HD, (h + 1) * HD)
                    acc = acc + _colsum(don[:, sl] * ns[h])
                    do_ref[:, sl] = _rms_bwd(don[:, sl] * gain, ns[h], rs[h]).astype(BF16)
                don_acc.append(acc)
            stat_ref[0:1, :] += dgate1
            stat_ref[1:2, :] += dpost1
            stat_ref[2:3, 0:HD] += don_acc[0]
            stat_ref[2:3, HD:2 * HD] += don_acc[1]

    lat = lambda i: (jnp.maximum(i - n_ctx_tiles, 0), 0)
    full = lambda i: (i, 0)
    fixed = lambda i: (0, 0)

    def pcol(blk):
        return pl.BlockSpec((TM, HW), lambda i: (i, blk))

    in_specs = ([pl.BlockSpec((TM, d), lat)] + [pl.BlockSpec((TM, HW), full)] * 4
                + [pcol(C_HGATE), pcol(C_GGATE), pcol(9), pcol(10), pcol(11), pcol(12)]
                + [pl.BlockSpec((TM, d), lat), pl.BlockSpec((TM, d), lat)]
                + [pl.BlockSpec((8, d), fixed)] * 3 + [VMEM_SPEC] * 3)
    f = lambda w: jax.ShapeDtypeStruct((total, w), BF16)
    out_shape = [f(HW), f(HW), f(HW), f(HW), f(d), f(d), jax.ShapeDtypeStruct((rows, d), BF16),
                 jax.ShapeDtypeStruct((rows, d), BF16), jax.ShapeDtypeStruct((rows, d), BF16),
                 jax.ShapeDtypeStruct((8, d), F32)]
    out_specs = ([pl.BlockSpec((TM, HW), full)] * 4 + [pl.BlockSpec((TM, d), full)] * 2
                 + [pl.BlockSpec((TM, d), lat)] * 3 + [pl.BlockSpec((8, d), fixed)])
    return pl.pallas_call(
        body, name=name, grid=(n_tiles,), out_shape=out_shape, in_specs=in_specs, out_specs=out_specs,
        compiler_params=_cparams(dimension_semantics=("arbitrary",)),
    )(x_lat, *o_list, p, p, p, p, p, p, dz2, y1, modx, norms, onorms, w_br_hg, w_br_gla, w_out)


def _in_projection_bwd(ctx0, x0, dz2, modc, modx, pre1, w_t, pieces, n_ctx_tiles, name):
    d = x0.shape[1]
    rows = ctx0.shape[0] + x0.shape[0]
    lat_rows = dz2.shape[0]
    width = P_WIDTH
    n_pieces = len(pieces)

    def body(*refs):
        ctx_ref, x_ref, dz2_ref, modc_ref, modx_ref, pre_ref, w_ref = refs[:7]
        (dhq_f, dhq_b, dhi_f, dhi_b, dhf_f, dhf_b, dhgate, dgq_f, dgq_b, dgk_f, dgk_b, dgv_f, dgv_b, dggate,
         dghg, dggla, dlr_f, dlr_b) = refs[7:7 + n_pieces]
        dp_ref, gx_ref, stat_ref = refs[7 + n_pieces:]
        i = pl.program_id(0)
        is_ctx = i < n_ctx_tiles
        z = jnp.where(is_ctx, ctx_ref[...], x_ref[...])
        sections = [
            (0, dhq_f[...] + dhq_b[...]), (HW, dhi_f[...] + dhi_b[...]), (2 * HW, dhf_f[...]), (3 * HW, dhf_b[...]),
            (4 * HW, dhgate[...]), (5 * HW, dgq_f[...] + dgq_b[...]), (6 * HW, dgk_f[...] + dgk_b[...]),
            (7 * HW, dgv_f[...] + dgv_b[...]), (8 * HW, dggate[...]),
            (9 * HW, dghg[:, 0:HW]), (10 * HW, dghg[:, HW:2 * HW]),
            (11 * HW, dggla[:, 0:HW]), (12 * HW, dggla[:, HW:2 * HW]), (OFF_LR, dlr_f[...] + dlr_b[...])]
        dh = jnp.zeros((TM, d), F32)
        for off, val in sections:
            w = val.shape[1]
            vb = val.astype(BF16)
            dp_ref[:, off:off + w] = vb
            dh = dh + jnp.dot(vb, w_ref[_w_in_row(off):_w_in_row(off) + w, :], preferred_element_type=F32)
        n, r = _rms(z)
        pre = pre_ref[...]
        scale = jnp.where(is_ctx, modc_ref[1:2, :], modx_ref[1:2, :])
        nw = n * pre
        dshift = _colsum(dh)
        dscale = _colsum(dh * nw)
        dnw = dh * (1.0 + scale)
        dpre = _colsum(dnw * n)
        gx_ref[...] = dz2_ref[...] + _rms_bwd(dnw * pre, n, r)
        zero = jnp.zeros((1, d), F32)

        @pl.when(i == 0)
        def _():
            stat_ref[...] = jnp.zeros_like(stat_ref)

        stat_ref[0:1, :] += jnp.where(is_ctx, zero, dshift)
        stat_ref[1:2, :] += jnp.where(is_ctx, zero, dscale)
        stat_ref[2:3, :] += jnp.where(is_ctx, dshift, zero)
        stat_ref[3:4, :] += jnp.where(is_ctx, dscale, zero)
        stat_ref[4:5, :] += dpre

    full = lambda i: (i, 0)
    lat = lambda i: (jnp.maximum(i - n_ctx_tiles, 0), 0)
    fixed = lambda i: (0, 0)
    piece_specs = [pl.BlockSpec((TM, a.shape[1]), full) for a in pieces]
    in_specs = [pl.BlockSpec((TM, d), lambda i: (jnp.minimum(i, n_ctx_tiles - 1), 0)), pl.BlockSpec((TM, d), lat),
                pl.BlockSpec((TM, d), lat), pl.BlockSpec((8, d), fixed),
                pl.BlockSpec((8, d), fixed), pl.BlockSpec((1, d), fixed), VMEM_SPEC] + piece_specs
    return pl.pallas_call(
        body, name=name, grid=(rows // TM,),
        out_shape=[jax.ShapeDtypeStruct((rows, width), BF16), jax.ShapeDtypeStruct((lat_rows, d), F32),
                   jax.ShapeDtypeStruct((8, d), F32)],
        in_specs=in_specs,
        out_specs=[pl.BlockSpec((TM, width), full), pl.BlockSpec((TM, d), lat), pl.BlockSpec((8, d), fixed)],
        compiler_params=_cparams(dimension_semantics=("arbitrary",)),
    )(ctx0, x0, dz2, modc, modx, pre1, w_t, *pieces)


def _transposed_lhs_matmul(x_ref, dy_ref, o_ref, xt_ref):
    @pl.when(pl.program_id(1) == 0)
    def _():
        xt_ref[...] = x_ref[...].T

    o_ref[...] = jnp.dot(xt_ref[...], dy_ref[...], preferred_element_type=F32)


def _w_in_grad(dp, h1, n_cols, name):
    rows, d = h1.shape
    n_main = OFF_LR // HW
    lr0 = _w_in_row(OFF_LR)

    def body(x_ref, xlr_ref, h_ref, o_hbm, xt_ref, acc_ref, sem):
        i = pl.program_id(0)

        def main_copy(step):
            row = jnp.where(step < 9, step * HW, step * HW + 2 * RANK)
            return pltpu.make_async_copy(acc_ref, o_hbm.at[pl.ds(pl.multiple_of(row, 8), HW), :], sem)

        lr_copy = pltpu.make_async_copy(acc_ref.at[0:2 * RANK, :], o_hbm.at[lr0:lr0 + 2 * RANK, :], sem)

        @pl.when(i < n_main)
        def _():
            xt_ref[...] = x_ref[...].T

        @pl.when(i > 0)
        def _():
            main_copy(i - 1).wait()

        @pl.when(i < n_main)
        def _():
            acc_ref[...] = jnp.dot(xt_ref[...], h_ref[...], preferred_element_type=F32)
            main_copy(i).start()

        @pl.when(i == n_main)
        def _():
            xt_ref[0:128, :] = xlr_ref[...].T
            acc_ref[0:128, :] = jnp.dot(xt_ref[0:128, :], h_ref[...], preferred_element_type=F32)
            lr_copy.start()
            lr_copy.wait()

    return pl.pallas_call(
        body, name=name, grid=(n_main + 1,),
        out_shape=jax.ShapeDtypeStruct((n_cols, d), F32),
        in_specs=[pl.BlockSpec((rows, HW), lambda i: (0, jnp.minimum(i, n_main - 1))),
                  pl.BlockSpec((rows, 128), lambda i: (0, OFF_LR // 128)),
                  pl.BlockSpec((rows, d), lambda i: (0, 0))],
        out_specs=ANY_SPEC,
        scratch_shapes=[pltpu.VMEM((HW, rows), BF16), pltpu.VMEM((HW, d), F32), pltpu.SemaphoreType.DMA],
        compiler_params=_cparams(dimension_semantics=("arbitrary",)),
    )(dp, dp, h1)


def _weight_grad(xs, dy, name, tk=None, tn=512, k_first=0, k_tiles=None):
    rows = dy.shape[0]
    n = dy.shape[1]
    tn_ = min(tn, n)
    tk_ = xs.shape[1] if tk is None else tk
    k_tiles = xs.shape[1] // tk_ if k_tiles is None else k_tiles
    k = k_tiles * tk_

    return pl.pallas_call(
        functools.partial(_transposed_lhs_matmul), name=name, grid=(k_tiles, n // tn_),
        out_shape=jax.ShapeDtypeStruct((k, n), F32),
        in_specs=[pl.BlockSpec((rows, tk_), lambda i, j: (0, i + k_first)),
                  pl.BlockSpec((rows, tn_), lambda i, j: (0, j))],
        out_specs=pl.BlockSpec((tk_, tn_), lambda i, j: (i, j)),
        scratch_shapes=[pltpu.VMEM((tk_, rows), BF16)],
        compiler_params=_cparams(dimension_semantics=("parallel", "arbitrary")),
    )(xs, dy)


def _running_sum(x, fw):
    c = x.shape[0]
    row = lax.broadcasted_iota(jnp.int32, (c, 1), 0)
    s = 1
    while s < c:
        if fw:
            x = x + jnp.where(row >= s, pltpu.roll(x, s, axis=0), 0.0)
        else:
            x = x + jnp.where(row < c - s, pltpu.roll(x, c - s, axis=0), 0.0)
        s *= 2
    return x


def _chunk_terms(q, k, g, fw):
    c = CHUNK
    r = lax.broadcasted_iota(jnp.int32, (c, c), 0)
    s = lax.broadcasted_iota(jnp.int32, (c, c), 1)
    causal = (s <= r) if fw else (s >= r)
    causal_t = (s >= r) if fw else (s <= r)
    cum = _running_sum(g, fw)
    row = lax.broadcasted_iota(jnp.int32, (c, 1), 0)
    pos = row if fw else (c - 1 - row)
    starts = [None]
    for j in range(1, NSUB):
        rj = SUB * j - 1 if fw else c - SUB * j
        starts.append(cum[rj:rj + 1, :])
    in_blk = [(pos >= SUB * j) & (pos < SUB * (j + 1)) for j in range(NSUB)]
    e = [jnp.exp(cum)]
    for j in range(1, NSUB):
        e.append(jnp.exp(jnp.where(pos >= SUB * j, cum - starts[j], -1e30)))
    own = jnp.zeros_like(cum)
    for j in range(1, NSUB):
        own = own + jnp.where(in_blk[j], starts[j], 0.0)
    kscale = jnp.exp(own - cum)
    rend = c - 1 if fw else 0
    cend = cum[rend:rend + 1, :]
    tail = jnp.exp(cend - cum)
    qcat = jnp.concatenate([q * e[j] for j in range(NSUB)], axis=1).astype(BF16)
    kt = k * kscale
    km = jnp.concatenate([jnp.where(in_blk[j], kt, 0.0) for j in range(NSUB)], axis=1).astype(BF16)
    return dict(causal=causal, causal_t=causal_t, e=e, in_blk=in_blk, kscale=kscale, cend=cend, tail=tail,
                qcat=qcat, km=km)


def _chunk_fwd(q, k, v, g, st0, fw):
    t = _chunk_terms(q, k, g, fw)
    a = jnp.where(t["causal"], _dot_nt(t["qcat"], t["km"]), 0.0)
    o = _dot(a, v) + _dot_nt(t["qcat"][:, 0:HD], st0)
    st1 = st0 * jnp.exp(t["cend"]) + _dot_tn(v, k * t["tail"])
    return o, st1


def _chunk_bwd(q, k, v, g, st0, do, dst1, fw):
    t = _chunk_terms(q, k, g, fw)
    qcat, km, e = t["qcat"], t["km"], t["e"]
    a_t = jnp.where(t["causal_t"], _dot_nt(km, qcat), 0.0)
    ktail = k * t["tail"]
    dv = _dot(a_t, do) + _dot_nt(ktail, dst1)
    da = jnp.where(t["causal"], _dot_nt(do, v), 0.0)
    da_t = jnp.where(t["causal_t"], _dot_nt(v, do), 0.0)
    dqcat = _dot(da, km)
    dq_inter = e[0] * _dot(do, st0)
    dq = dq_inter
    for j in range(NSUB):
        dq = dq + e[j] * dqcat[:, j * HD:(j + 1) * HD]
    dkm = _dot(da_t, qcat)
    dkt = jnp.zeros_like(k)
    for j in range(NSUB):
        dkt = dkt + jnp.where(t["in_blk"][j], dkm[:, j * HD:(j + 1) * HD], 0.0)
    dk_inter = _dot(v, dst1) * t["tail"]
    dk = dkt * t["kscale"] + dk_inter
    dcum = q * dq_inter - k * dk_inter
    for j in range(NSUB):
        sl = slice(j * HD, (j + 1) * HD)
        dcum = dcum + qcat[:, sl].astype(F32) * dqcat[:, sl] - km[:, sl].astype(F32) * dkm[:, sl]
    ecend = jnp.exp(t["cend"])
    end = ecend * _colsum(st0 * dst1) + _colsum(k * dk_inter)
    dg = _running_sum(dcum, not fw) + end
    dst0 = dst1 * ecend + _dot_tn(do, q * e[0])
    return dq, dk, dv, dg, dst0


def _running_sums(xs, fws):
    c = xs[0].shape[0]
    row = lax.broadcasted_iota(jnp.int32, (c, 1), 0)
    s = 1
    while s < c:
        xs = [x + (jnp.where(row >= s, pltpu.roll(x, s, axis=0), 0.0) if fw else
                   jnp.where(row < c - s, pltpu.roll(x, c - s, axis=0), 0.0)) for x, fw in zip(xs, fws)]
        s *= 2
    return xs


def _chunks_terms(qs, ks, gs, fws):
    c = CHUNK
    n = len(qs)
    r = lax.broadcasted_iota(jnp.int32, (c, c), 0)
    s = lax.broadcasted_iota(jnp.int32, (c, c), 1)
    row = lax.broadcasted_iota(jnp.int32, (c, 1), 0)
    per_dir = {}
    for fw in set(fws):
        pos = row if fw else (c - 1 - row)
        per_dir[fw] = dict(
            causal=(s <= r) if fw else (s >= r), causal_t=(s >= r) if fw else (s <= r), pos=pos,
            in_blk=[(pos >= SUB * j) & (pos < SUB * (j + 1)) for j in range(NSUB)],
            start_row=[None] + [SUB * j - 1 if fw else c - SUB * j for j in range(1, NSUB)],
            rend=c - 1 if fw else 0)
    dirs = [per_dir[fw] for fw in fws]
    cums = _running_sums(gs, fws)
    starts = [[None] + [cum[d["start_row"][j]:d["start_row"][j] + 1, :] for j in range(1, NSUB)]
              for cum, d in zip(cums, dirs)]
    es = [[jnp.exp(cum) for cum in cums]]
    for j in range(1, NSUB):
        es.append([jnp.exp(jnp.where(d["pos"] >= SUB * j, cum - st[j], -1e30)) for cum, st, d in zip(cums, starts, dirs)])
    owns = [sum(jnp.where(d["in_blk"][j], st[j], 0.0) for j in range(1, NSUB)) for st, d in zip(starts, dirs)]
    kscales = [jnp.exp(own - cum) for own, cum in zip(owns, cums)]
    cends = [cum[d["rend"]:d["rend"] + 1, :] for cum, d in zip(cums, dirs)]
    tails = [jnp.exp(cend - cum) for cend, cum in zip(cends, cums)]
    qcats = [jnp.concatenate([q * es[j][i] for j in range(NSUB)], axis=1).astype(BF16) for i, q in enumerate(qs)]
    kts = [k * ksc for k, ksc in zip(ks, kscales)]
    kms = [jnp.concatenate([jnp.where(d["in_blk"][j], kt, 0.0) for j in range(NSUB)], axis=1).astype(BF16)
           for kt, d in zip(kts, dirs)]
    e_by_lane = [[es[j][i] for j in range(NSUB)] for i in range(n)]
    return dict(dirs=dirs, e=e_by_lane, kscale=kscales, cend=cends, tail=tails, qcat=qcats, km=kms)


def _chunks_fwd(qs, ks, vs, gs, st0s, fws):
    t = _chunks_terms(qs, ks, gs, fws)
    scores = [_dot_nt(qc, km) for qc, km in zip(t["qcat"], t["km"])]
    a = [jnp.where(d["causal"], sc, 0.0) for sc, d in zip(scores, t["dirs"])]
    inter = [_dot_nt(qc[:, 0:HD], st0) for qc, st0 in zip(t["qcat"], st0s)]
    intra = [_dot(a_, v) for a_, v in zip(a, vs)]
    os_ = [x + y for x, y in zip(intra, inter)]
    upd = [_dot_tn(v, k * tl) for v, k, tl in zip(vs, ks, t["tail"])]
    st1s = [st0 * jnp.exp(ce) + u for st0, ce, u in zip(st0s, t["cend"], upd)]
    return os_, st1s


def _chunks_bwd(qs, ks, vs, gs, st0s, dos, dst1s, fws):
    n = len(qs)
    t = _chunks_terms(qs, ks, gs, fws)
    qcat, km, e, dirs = t["qcat"], t["km"], t["e"], t["dirs"]
    a_t = [jnp.where(d["causal_t"], _dot_nt(km_, qc), 0.0) for km_, qc, d in zip(km, qcat, dirs)]
    ktail = [k * tl for k, tl in zip(ks, t["tail"])]
    dv_a = [_dot(at, do) for at, do in zip(a_t, dos)]
    dv_b = [_dot_nt(kt, ds) for kt, ds in zip(ktail, dst1s)]
    dv = [x + y for x, y in zip(dv_a, dv_b)]
    da = [jnp.where(d["causal"], _dot_nt(do, v), 0.0) for do, v, d in zip(dos, vs, dirs)]
    da_t = [jnp.where(d["causal_t"], _dot_nt(v, do), 0.0) for do, v, d in zip(dos, vs, dirs)]
    dqcat = [_dot(da_, km_) for da_, km_ in zip(da, km)]
    dq_inter = [e[i][0] * _dot(dos[i], st0s[i]) for i in range(n)]
    dkm = [_dot(dat, qc) for dat, qc in zip(da_t, qcat)]
    dk_inter = [_dot(v, ds) * tl for v, ds, tl in zip(vs, dst1s, t["tail"])]
    dq = [dq_inter[i] + sum(e[i][j] * dqcat[i][:, j * HD:(j + 1) * HD] for j in range(NSUB)) for i in range(n)]
    dkt = [sum(jnp.where(dirs[i]["in_blk"][j], dkm[i][:, j * HD:(j + 1) * HD], 0.0) for j in range(NSUB))
           for i in range(n)]
    dk = [dkt[i] * t["kscale"][i] + dk_inter[i] for i in range(n)]
    dcum = [qs[i] * dq_inter[i] - ks[i] * dk_inter[i]
            + sum(qcat[i][:, j * HD:(j + 1) * HD].astype(F32) * dqcat[i][:, j * HD:(j + 1) * HD]
                  - km[i][:, j * HD:(j + 1) * HD].astype(F32) * dkm[i][:, j * HD:(j + 1) * HD] for j in range(NSUB))
            for i in range(n)]
    ecend = [jnp.exp(ce) for ce in t["cend"]]
    end = [ecend[i] * _colsum(st0s[i] * dst1s[i]) + _colsum(ks[i] * dk_inter[i]) for i in range(n)]
    sums = _running_sums(dcum, [not fw for fw in fws])
    dg = [sm + en for sm, en in zip(sums, end)]
    upd = [_dot_tn(dos[i], qs[i] * e[i][0]) for i in range(n)]
    dst0 = [dst1s[i] * ecend[i] + upd[i] for i in range(n)]
    return dq, dk, dv, dg, dst0


def _chunk_index(step, n_ctx_chunks, n_chunks, fw):
    if fw:
        return step
    return jnp.where(step < n_ctx_chunks, n_ctx_chunks - 1 - step, n_chunks - 1 + n_ctx_chunks - step)


def _hg_inputs(hq, hf, lbv, d_idx, sl):
    lb = _sigmoid(lbv[d_idx:d_idx + 1, sl] - lbv[2 + d_idx:3 + d_idx, sl])
    sg = _sigmoid(hf)
    f = lb + (1.0 - lb) * sg
    return _silu(hq), 1.0 - f, jnp.log(f), f, sg, lb


def _scan_fwd(p, side, n_ctx_chunks, fw, branch, name):
    rows = p.shape[0]
    n_chunks = rows // CHUNK
    d_idx = 0 if fw else 1
    hg = branch == "hg"
    cols = (C_HQ, C_HI, C_HF_FW + d_idx) if hg else (C_GQ, C_GK, C_GV)

    def body(*refs):
        if hg:
            a_ref, b_ref, c_ref, lb_ref, o_ref, st_ref, state = refs
        else:
            a_ref, b_ref, c_ref, lr_ref, wgk_ref, bgk_ref, o_ref, st_ref, state = refs
            logits = _dot(lr_ref[...], wgk_ref[...]) + bgk_ref[...]
            g_all = _log_sigmoid(logits) * (1.0 / GATE_NORM)

        @pl.when(pl.program_id(0) == 0)
        def _():
            state[...] = jnp.zeros_like(state)

        for h in range(NH):
            sl = slice(h * HD, (h + 1) * HD)
            if hg:
                q, k, g, _, _, _ = _hg_inputs(a_ref[:, sl], c_ref[:, sl], lb_ref[...], d_idx, sl)
                v = b_ref[:, sl]
            else:
                q, k, v, g = a_ref[:, sl] * (HD ** -0.5), b_ref[:, sl], c_ref[:, sl], g_all[:, sl]
            st0 = state[h]
            st_ref[0, h] = st0
            o, st1 = _chunk_fwd(q, k, v, g, st0, fw)
            o_ref[:, sl] = o
            state[h] = st1

    def cmap(blk):
        return pl.BlockSpec((CHUNK, HW), lambda j: (_chunk_index(j, n_ctx_chunks, n_chunks, fw), blk))

    fixed = lambda j: (0, 0)
    in_specs = [cmap(cols[0]), cmap(cols[1]), cmap(cols[2])]
    if hg:
        in_specs += [pl.BlockSpec((4, HW), fixed)]
        args = (p, p, p, side)
    else:
        in_specs += [pl.BlockSpec((CHUNK, 128), lambda j: (_chunk_index(j, n_ctx_chunks, n_chunks, fw), OFF_LR // 128)),
                     pl.BlockSpec((128, HW), fixed), pl.BlockSpec((1, HW), fixed)]
        args = (p, p, p, p, side[0], side[1])
    return pl.pallas_call(
        body, name=name, grid=(n_chunks,),
        out_shape=[jax.ShapeDtypeStruct((rows, HW), F32), jax.ShapeDtypeStruct((n_chunks, NH, HD, HD), F32)],
        in_specs=in_specs,
        out_specs=[pl.BlockSpec((CHUNK, HW), lambda j: (_chunk_index(j, n_ctx_chunks, n_chunks, fw), 0)),
                   pl.BlockSpec((1, NH, HD, HD), lambda j: (_chunk_index(j, n_ctx_chunks, n_chunks, fw), 0, 0, 0))],
        scratch_shapes=[pltpu.VMEM((NH, HD, HD), F32)],
        compiler_params=_cparams(dimension_semantics=("arbitrary",)),
    )(*args)


def _scan_fwd_both(p, side, n_ctx_chunks, branch, name):
    rows = p.shape[0]
    n_chunks = rows // CHUNK
    hg = branch == "hg"
    n_in = 4 if hg else 6

    def body(*refs):
        ins, outs, state = refs[:2 * n_in], refs[2 * n_in:2 * n_in + 4], refs[-1]

        @pl.when(pl.program_id(0) == 0)
        def _():
            state[...] = jnp.zeros_like(state)

        lanes, where = [], []
        for di, fw in enumerate((True, False)):
            r = ins[di * n_in:(di + 1) * n_in]
            o_ref, st_ref = outs[2 * di], outs[2 * di + 1]
            if hg:
                a_ref, b_ref, c_ref, lb_ref = r
            else:
                a_ref, b_ref, c_ref, lr_ref, wgk_ref, bgk_ref = r
                logits = _dot(lr_ref[...], wgk_ref[...]) + bgk_ref[...]
                g_all = _log_sigmoid(logits) * (1.0 / GATE_NORM)
            for h in range(NH):
                sl = slice(h * HD, (h + 1) * HD)
                if hg:
                    q, k, g, _, _, _ = _hg_inputs(a_ref[:, sl], c_ref[:, sl], lb_ref[...], di, sl)
                    v = b_ref[:, sl]
                else:
                    q, k, v, g = a_ref[:, sl] * (HD ** -0.5), b_ref[:, sl], c_ref[:, sl], g_all[:, sl]
                lanes.append((q, k, v, g, state[di, h], fw))
                where.append((di, h, sl, o_ref, st_ref))
        qs, ks, vs, gs, st0s, fws = (list(col) for col in zip(*lanes))
        os_, st1s = _chunks_fwd(qs, ks, vs, gs, st0s, fws)
        for (di, h, sl, o_ref, st_ref), st0, o, st1 in zip(where, st0s, os_, st1s):
            st_ref[0, h] = st0
            o_ref[:, sl] = o
            state[di, h] = st1

    fixed = lambda j: (0, 0)
    in_specs, args, out_specs = [], [], []
    for di, fw in enumerate((True, False)):
        chunk = functools.partial(_chunk_index, n_ctx_chunks=n_ctx_chunks, n_chunks=n_chunks, fw=fw)

        def cmap(blk, width=HW, chunk=chunk):
            return pl.BlockSpec((CHUNK, width), lambda j: (chunk(j), blk))

        if hg:
            in_specs += [cmap(C_HQ), cmap(C_HI), cmap(C_HF_FW + di), pl.BlockSpec((4, HW), fixed)]
            args += [p, p, p, side]
        else:
            in_specs += [cmap(C_GQ), cmap(C_GK), cmap(C_GV), cmap(OFF_LR // 128, 128),
                         pl.BlockSpec((128, HW), fixed), pl.BlockSpec((1, HW), fixed)]
            args += [p, p, p, p, side[di][0], side[di][1]]
        out_specs += [cmap(0), pl.BlockSpec((1, NH, HD, HD), lambda j, chunk=chunk: (chunk(j), 0, 0, 0))]
    return pl.pallas_call(
        body, name=name, grid=(n_chunks,),
        out_shape=[jax.ShapeDtypeStruct((rows, HW), F32), jax.ShapeDtypeStruct((n_chunks, NH, HD, HD), F32)] * 2,
        in_specs=in_specs, out_specs=out_specs,
        scratch_shapes=[pltpu.VMEM((2, NH, HD, HD), F32)],
        compiler_params=_cparams(dimension_semantics=("arbitrary",)),
    )(*args)


def _scan_bwd_both(p, side, states, d_o, n_ctx_chunks, branch, name):
    rows = p.shape[0]
    n_chunks = rows // CHUNK
    hg = branch == "hg"
    n_in = 6 if hg else 8
    n_out = 4 if hg else 6

    def body(*refs):
        ins, outs, dstate = refs[:2 * n_in], refs[2 * n_in:2 * n_in + 2 * n_out], refs[-1]
        first = pl.program_id(0) == 0

        @pl.when(first)
        def _():
            dstate[...] = jnp.zeros_like(dstate)

        lanes, where, extra, ctx = [], [], [], []
        for di, fw in enumerate((True, False)):
            r, w = ins[di * n_in:(di + 1) * n_in], outs[di * n_out:(di + 1) * n_out]
            if hg:
                a_ref, b_ref, c_ref, lb_ref, st_ref, do_ref = r
                da_ref, db_ref, dc_ref, dlb_ref = w
                acc_refs = (dlb_ref,)
            else:
                a_ref, b_ref, c_ref, lr_ref, wgk_ref, bgk_ref, st_ref, do_ref = r
                da_ref, db_ref, dc_ref, dlr_ref, dwgk_ref, dbias_ref = w
                acc_refs = (dwgk_ref, dbias_ref)
                lr = lr_ref[...]
                logits = _dot(lr, wgk_ref[...]) + bgk_ref[...]
                g_all = _log_sigmoid(logits) * (1.0 / GATE_NORM)

            @pl.when(first)
            def _(acc_refs=acc_refs):
                for ref in acc_refs:
                    ref[...] = jnp.zeros_like(ref)

            for h in range(NH):
                sl = slice(h * HD, (h + 1) * HD)
                if hg:
                    hq, hf = a_ref[:, sl], c_ref[:, sl]
                    q, k, g, f, sg, lb = _hg_inputs(hq, hf, lb_ref[...], di, sl)
                    v = b_ref[:, sl]
                    extra.append((hq, f, sg, lb))
                else:
                    q, k, v, g = a_ref[:, sl] * (HD ** -0.5), b_ref[:, sl], c_ref[:, sl], g_all[:, sl]
                    extra.append(None)
                lanes.append((q, k, v, g, st_ref[0, h], do_ref[:, sl], dstate[di, h], fw))
                where.append((di, h, sl))
            ctx.append((w, None if hg else (lr, logits, wgk_ref)))

        qs, ks, vs, gs, st0s, dos, dst1s, fws = (list(col) for col in zip(*lanes))
        dqs, dks, dvs, dgs, dst0s = _chunks_bwd(qs, ks, vs, gs, st0s, dos, dst1s, fws)
        dg_parts = {0: [], 1: []}
        for (di, h, sl), ex, dq, dk, dv, dg, dst0 in zip(where, extra, dqs, dks, dvs, dgs, dst0s):
            dstate[di, h] = dst0
            w = ctx[di][0]
            if hg:
                hq, f, sg, lb = ex
                da_ref, db_ref, dc_ref, dlb_ref = w
                da_ref[:, sl] = (dq * _dsilu(hq)).astype(BF16)
                db_ref[:, sl] = dv.astype(BF16)
                df = dg / f - dk
                dc_ref[:, sl] = (df * (1.0 - lb) * sg * (1.0 - sg)).astype(BF16)
                dlb_ref[0:1, sl] += _colsum(df * (1.0 - sg))
            else:
                da_ref, db_ref, dc_ref = w[:3]
                da_ref[:, sl] = (dq * (HD ** -0.5)).astype(BF16)
                db_ref[:, sl] = dk.astype(BF16)
                dc_ref[:, sl] = dv.astype(BF16)
                dg_parts[di].append(dg)
        if not hg:
            for di in range(2):
                dlr_ref, dwgk_ref, dbias_ref = ctx[di][0][3:]
                lr, logits, wgk_ref = ctx[di][1]
                dlogits = jnp.concatenate(dg_parts[di], axis=1) * (1.0 / GATE_NORM) * (1.0 - _sigmoid(logits))
                dlr_ref[...] = _dot_nt(dlogits, wgk_ref[...]).astype(BF16)
                dwgk_ref[...] += _dot_tn(lr, dlogits)
                dbias_ref[0:1, :] += _colsum(dlogits)

    fixed = lambda j: (0, 0)
    big = jax.ShapeDtypeStruct((rows, HW), BF16)
    in_specs, args, out_shape, out_specs = [], [], [], []
    for di, fw in enumerate((True, False)):
        def chunk_of(j, fw=fw):
            return _chunk_index(n_chunks - 1 - j, n_ctx_chunks, n_chunks, fw)

        def cmap(blk, width=HW, chunk_of=chunk_of):
            return pl.BlockSpec((CHUNK, width), lambda j: (chunk_of(j), blk))

        st_spec = pl.BlockSpec((1, NH, HD, HD), lambda j, chunk_of=chunk_of: (chunk_of(j), 0, 0, 0))
        if hg:
            in_specs += [cmap(C_HQ), cmap(C_HI), cmap(C_HF_FW + di), pl.BlockSpec((4, HW), fixed), st_spec, cmap(0)]
            args += [p, p, p, side, states[di], d_o]
            out_shape += [big, big, big, jax.ShapeDtypeStruct((8, HW), F32)]
            out_specs += [cmap(0), cmap(0), cmap(0), pl.BlockSpec((8, HW), fixed)]
        else:
            in_specs += [cmap(C_GQ), cmap(C_GK), cmap(C_GV), cmap(OFF_LR // 128, 128),
                         pl.BlockSpec((128, HW), fixed), pl.BlockSpec((1, HW), fixed), st_spec, cmap(0)]
            args += [p, p, p, p, side[di][0], side[di][1], states[di], d_o]
            out_shape += [big, big, big, jax.ShapeDtypeStruct((rows, 128), BF16),
                          jax.ShapeDtypeStruct((128, HW), F32), jax.ShapeDtypeStruct((8, HW), F32)]
            out_specs += [cmap(0), cmap(0), cmap(0), cmap(0, 128), pl.BlockSpec((128, HW), fixed),
                          pl.BlockSpec((8, HW), fixed)]
    return pl.pallas_call(
        body, name=name, grid=(n_chunks,), out_shape=out_shape, in_specs=in_specs, out_specs=out_specs,
        scratch_shapes=[pltpu.VMEM((2, NH, HD, HD), F32)],
        compiler_params=_cparams(dimension_semantics=("arbitrary",)),
    )(*args)


def _scan_bwd(p, side, states, d_o, n_ctx_chunks, fw, branch, name):
    rows = p.shape[0]
    n_chunks = rows // CHUNK
    d_idx = 0 if fw else 1
    hg = branch == "hg"
    cols = (C_HQ, C_HI, C_HF_FW + d_idx) if hg else (C_GQ, C_GK, C_GV)

    def body(*refs):
        if hg:
            a_ref, b_ref, c_ref, lb_ref, st_ref, do_ref, da_ref, db_ref, dc_ref, dlb_ref, dstate = refs
        else:
            (a_ref, b_ref, c_ref, lr_ref, wgk_ref, bgk_ref, st_ref, do_ref, da_ref, db_ref, dc_ref, dlr_ref,
             dwgk_ref, dbias_ref, dstate) = refs
            lr = lr_ref[...]
            logits = _dot(lr, wgk_ref[...]) + bgk_ref[...]
            g_all = _log_sigmoid(logits) * (1.0 / GATE_NORM)

        @pl.when(pl.program_id(0) == 0)
        def _():
            dstate[...] = jnp.zeros_like(dstate)
            if hg:
                dlb_ref[...] = jnp.zeros_like(dlb_ref)
            else:
                dwgk_ref[...] = jnp.zeros_like(dwgk_ref)
                dbias_ref[...] = jnp.zeros_like(dbias_ref)

        dg_parts = []
        for h in range(NH):
            sl = slice(h * HD, (h + 1) * HD)
            if hg:
                hq, hf = a_ref[:, sl], c_ref[:, sl]
                q, k, g, f, sg, lb = _hg_inputs(hq, hf, lb_ref[...], d_idx, sl)
                v = b_ref[:, sl]
            else:
                q, k, v, g = a_ref[:, sl] * (HD ** -0.5), b_ref[:, sl], c_ref[:, sl], g_all[:, sl]
            dq, dk, dv, dg, dst0 = _chunk_bwd(q, k, v, g, st_ref[0, h], do_ref[:, sl], dstate[h], fw)
            dstate[h] = dst0
            if hg:
                da_ref[:, sl] = dq * _dsilu(hq)
                db_ref[:, sl] = dv
                df = dg / f - dk
                dc_ref[:, sl] = df * (1.0 - lb) * sg * (1.0 - sg)
                dlb_ref[0:1, sl] += _colsum(df * (1.0 - sg))
            else:
                da_ref[:, sl] = dq * (HD ** -0.5)
                db_ref[:, sl] = dk
                dc_ref[:, sl] = dv
                dg_parts.append(dg)
        if not hg:
            dlogits = jnp.concatenate(dg_parts, axis=1) * (1.0 / GATE_NORM) * (1.0 - _sigmoid(logits))
            dlr_ref[...] = _dot_nt(dlogits, wgk_ref[...])
            dwgk_ref[...] += _dot_tn(lr, dlogits)
            dbias_ref[0:1, :] += _colsum(dlogits)

    def chunk_of(j):
        return _chunk_index(n_chunks - 1 - j, n_ctx_chunks, n_chunks, fw)

    def cmap(blk, width=HW):
        return pl.BlockSpec((CHUNK, width), lambda j: (chunk_of(j), blk))

    fixed = lambda j: (0, 0)
    st_spec = pl.BlockSpec((1, NH, HD, HD), lambda j: (chunk_of(j), 0, 0, 0))
    big = jax.ShapeDtypeStruct((rows, HW), F32)
    if hg:
        in_specs = [cmap(cols[0]), cmap(cols[1]), cmap(cols[2]), pl.BlockSpec((4, HW), fixed), st_spec, cmap(0)]
        args = (p, p, p, side, states, d_o)
        out_shape = [big, big, big, jax.ShapeDtypeStruct((8, HW), F32)]
        out_specs = [cmap(0), cmap(0), cmap(0), pl.BlockSpec((8, HW), fixed)]
    else:
        in_specs = [cmap(cols[0]), cmap(cols[1]), cmap(cols[2]), cmap(OFF_LR // 128, 128),
                    pl.BlockSpec((128, HW), fixed), pl.BlockSpec((1, HW), fixed), st_spec, cmap(0)]
        args = (p, p, p, p, side[0], side[1], states, d_o)
        out_shape = [big, big, big, jax.ShapeDtypeStruct((rows, 128), F32), jax.ShapeDtypeStruct((128, HW), F32),
                     jax.ShapeDtypeStruct((8, HW), F32)]
        out_specs = [cmap(0), cmap(0), cmap(0), cmap(0, 128), pl.BlockSpec((128, HW), fixed),
                     pl.BlockSpec((8, HW), fixed)]
    return pl.pallas_call(
        body, name=name, grid=(n_chunks,), out_shape=out_shape, in_specs=in_specs, out_specs=out_specs,
        scratch_shapes=[pltpu.VMEM((NH, HD, HD), F32)],
        compiler_params=_cparams(dimension_semantics=("arbitrary",)),
    )(*args)


SMALL_ROWS = 56
ROWS_MOD_X = (0, 1, 8, 16, 17, 18)
ROWS_MOD_C = (2, 3)
ROW_PRE1, ROW_POST1, ROW_ONORM, ROW_PRE2, ROW_POST2, ROW_LB, ROW_BGK, ROW_WGK = 4, 9, 10, 19, 20, 24, 32, 40
ROW_LOSS = 21


def _reduce_small(gathered, lb_full, name):
    _, _, d = gathered.shape

    def body(g_ref, lb_ref, sum_ref, dmod_ref, dbmod_ref, dlb_ref):
        total = g_ref[0]
        for b in range(1, N_DEV):
            total = total + g_ref[b]
        sum_ref[...] = total
        dmod_ref[...] = jnp.zeros_like(dmod_ref)
        for m in range(N_MOD):
            col = slice(m * d, (m + 1) * d)
            acc = jnp.zeros((1, d), F32)
            for b in range(N_DEV):
                row = g_ref[b, ROWS_MOD_X[m]:ROWS_MOD_X[m] + 1, :]
                dmod_ref[b:b + 1, col] = row
                acc = acc + row
            if m < 2:
                ctx_row = total[ROWS_MOD_C[m]:ROWS_MOD_C[m] + 1, :]
                dmod_ref[8:9, col] = ctx_row
                acc = acc + ctx_row
            dbmod_ref[:, col] = acc
        lbv = lb_ref[...]
        for dd in range(2):
            lb = _sigmoid(lbv[dd:dd + 1, :] - lbv[2 + dd:3 + dd, :])
            gl = total[ROW_LB:ROW_LB + 1, dd * HW:(dd + 1) * HW] * lb * (1.0 - lb)
            dlb_ref[dd:dd + 1, :] = gl
            dlb_ref[2 + dd:3 + dd, :] = -gl

    return pl.pallas_call(
        body, name=name,
        out_shape=[jax.ShapeDtypeStruct((SMALL_ROWS, d), F32), jax.ShapeDtypeStruct((16, N_MOD * d), F32),
                   jax.ShapeDtypeStruct((1, N_MOD * d), F32), jax.ShapeDtypeStruct((4, HW), F32)],
        in_specs=[VMEM_SPEC] * 2, out_specs=[VMEM_SPEC] * 4, compiler_params=_cparams(),
    )(gathered, lb_full)


def _c_ctx_grad(gathered, c_ctx_row, name):
    def body(g_ref, c_ref, o_ref):
        acc = g_ref[0, 0:1, :]
        for chip in range(1, N_CHIP):
            acc = acc + g_ref[2 * chip, 0:1, :]
        o_ref[...] = acc * _dsilu(c_ref[...])

    return pl.pallas_call(
        body, name=name, out_shape=jax.ShapeDtypeStruct(c_ctx_row.shape, F32),
        in_specs=[VMEM_SPEC] * 2, out_specs=VMEM_SPEC, compiler_params=_cparams(),
    )(gathered, c_ctx_row)


def _relayout_w_in(w):
    pad = jnp.zeros((w.shape[0], 128 - 2 * RANK), w.dtype)
    return jnp.concatenate([w[:, :9 * HW], w[:, 9 * HW + 2 * RANK:], w[:, 9 * HW:9 * HW + 2 * RANK], pad], axis=1)


def _relayout_w_in_rows(wt):
    pad = jnp.zeros((128 - 2 * RANK, wt.shape[1]), wt.dtype)
    return jnp.concatenate([wt[:9 * HW], wt[9 * HW + 2 * RANK:], wt[9 * HW:9 * HW + 2 * RANK], pad], axis=0)


def _w_in_grad_rows(g_main, g_lr):
    return jnp.concatenate([g_main[:9 * HW], g_lr[:2 * RANK], g_main[9 * HW:]], axis=0)


def _w_in_grad_blocks(g_main, g_lr, n_blocks):
    lr0 = 9 * HW
    n = (g_main.shape[1] + 2 * RANK) // n_blocks

    def cols(lo, hi):
        out = []
        if lo < lr0:
            out.append(g_main[:, lo:min(hi, lr0)])
        if hi > lr0 and lo < lr0 + 2 * RANK:
            out.append(g_lr[:, max(lo, lr0) - lr0:min(hi, lr0 + 2 * RANK) - lr0])
        if hi > lr0 + 2 * RANK:
            out.append(g_main[:, max(lo, lr0 + 2 * RANK) - 2 * RANK:hi - 2 * RANK])
        return out

    return jnp.stack([jnp.concatenate(cols(j * n, (j + 1) * n), axis=1) for j in range(n_blocks)])


def _blocked(full, n_blocks):
    k, n = full.shape
    return full.reshape(k, n_blocks, n // n_blocks).transpose(1, 0, 2)


def _unblocked(blocks):
    nb, k, n = blocks.shape
    return blocks.transpose(1, 0, 2).reshape(k, nb * n)


def _sample_front(x0, ctx0, modc, modx, norm_pre1, lb_full, gla_side, w_in_r):
    ctx_len = ctx0.shape[0]
    n_ctx_tiles = ctx_len // TM
    n_ctx_chunks = ctx_len // CHUNK
    h1, p = _in_projection(ctx0, x0, modc, modx, norm_pre1, w_in_r, n_ctx_tiles, "in_projection")
    o_hg_fw, st_hg_fw, o_hg_bw, st_hg_bw = _scan_fwd_both(p, lb_full, n_ctx_chunks, "hg", "scan_hg")
    o_gla_fw, st_gla_fw, o_gla_bw, st_gla_bw = _scan_fwd_both(p, gla_side, n_ctx_chunks, "gla", "scan_gla")
    return dict(h1=h1, p=p, o_list=[o_hg_fw, o_hg_bw, o_gla_fw, o_gla_bw],
                states=[st_hg_fw, st_hg_bw, st_gla_fw, st_gla_bw])


def _sample_back(reduce, front, x0, ctx0, target0, modc, modx, norm_pre1, norms, onorms, lb_full, gla_side, w_in_r,
                 wbh, wbg, wout, ffn_weights):
    seq, d = x0.shape
    ctx_len = ctx0.shape[0]
    n_ctx_tiles = ctx_len // TM
    n_tiles = (ctx_len + seq) // TM
    n_ctx_chunks = ctx_len // CHUNK
    h1, p, o_list = front["h1"], front["p"], front["o_list"]
    st_hg_fw, st_hg_bw, st_gla_fw, st_gla_bw = front["states"]
    z2, y1, merged, og_hg, og_gla = _mixer_tail_fwd(x0, p, o_list, modx, norms, onorms, wbh, wbg, wout, n_ctx_tiles,
                                                    "mixer_tail")
    wg, wu, wd = ffn_weights([z2])
    loss_part, dz2, h2, a_act, du, dv, dy2, stat_ffn = _ffn_fwd_bwd(z2, modx, norms, wg, wu, wd, target0, "ffn")
    dff = wg.shape[0]
    tok = reduce("ffn", [_weight_grad(du, h2, "grad_w_ff_gate", tk=dff // 2, tn=d),
                         _weight_grad(dv, h2, "grad_w_ff_up", tk=dff // 2, tn=d),
                         _weight_grad(a_act, dy2, "grad_w_ff_down", tk=dff // 2)])

    (d_ohg, d_ogla, d_hgate, d_ggate, d_ghg, d_ggla, dy1, db_hg, db_gla, stat_mix) = _mixer_tail_bwd(
        x0, p, o_list, dz2, y1, modx + tok, norms, onorms, wbh, wbg, wout, n_ctx_tiles, n_tiles, "mixer_tail_bwd")
    tok = reduce("mix", [_weight_grad(og_hg, db_hg, "grad_w_br_hg"), _weight_grad(og_gla, db_gla, "grad_w_br_gla"),
                         _weight_grad(merged, dy1, "grad_w_out")])
    tok = tok + reduce("push_ffn", [dy1])
    gla_b = [(wgk, bias + tok) for wgk, bias in gla_side]
    (dgq_f, dgk_f, dgv_f, dlr_f, dwgk_f, dbgk_f, dgq_b, dgk_b, dgv_b, dlr_b, dwgk_b, dbgk_b) = _scan_bwd_both(
        p, gla_b, (st_gla_fw, st_gla_bw), d_ogla, n_ctx_chunks, "gla", "scan_gla_bwd")
    lb_b = lb_full + reduce("push_mix", [dbgk_f])
    (dhq_f, dhi_f, dhf_f, dlb_f, dhq_b, dhi_b, dhf_b, dlb_b) = _scan_bwd_both(
        p, lb_b, (st_hg_fw, st_hg_bw), d_ohg, n_ctx_chunks, "hg", "scan_hg_bwd")
    pieces = [dhq_f, dhq_b, dhi_f, dhi_b, dhf_f, dhf_b, d_hgate, dgq_f, dgq_b, dgk_f, dgk_b, dgv_f, dgv_b, d_ggate,
              d_ghg, d_ggla, dlr_f, dlr_b]
    dp, grad_x, stat_in = _in_projection_bwd(ctx0, x0, dz2, modc, modx, norm_pre1, w_in_r, pieces, n_ctx_tiles,
                                             "in_projection_bwd")

    tok = reduce("in", [_w_in_grad(dp, h1, w_in_r.shape[0], "grad_w_in")])
    reduce("small", dict(stat_in=stat_in + tok, stat_mix=stat_mix, stat_ffn=stat_ffn, dlb=(dlb_f, dlb_b),
                         dwgk=(dwgk_f, dwgk_b), dbgk=(dbgk_f, dbgk_b)))
    reduce("push_in", [])
    return dict(
        loss_part=loss_part, grad_x=grad_x, stat_in=stat_in, stat_mix=stat_mix, stat_ffn=stat_ffn,
        dlb=(dlb_f, dlb_b), dwgk=(dwgk_f, dwgk_b), dbgk=(dbgk_f, dbgk_b))


def kernel(x, c, ctx, c_ctx, w_mod, b_mod, norm_pre1, norm_post1, norm_pre2, norm_post2, w_in, hg_lb, hg_onorm, gla_w_gk, gla_b_gk, gla_onorm, w_br_hg, w_br_gla, w_out, w_ff_gate, w_ff_up, w_ff_down, loss_target, m_c_ctx, m_w_mod, m_b_mod, m_norm_pre1, m_norm_post1, m_norm_pre2, m_norm_post2, m_w_in, m_hg_lb, m_hg_onorm, m_gla_w_gk, m_gla_b_gk, m_gla_onorm, m_w_br_hg, m_w_br_gla, m_w_out, m_w_ff_gate, m_w_ff_up, m_w_ff_down, v_c_ctx, v_w_mod, v_b_mod, v_norm_pre1, v_norm_post1, v_norm_pre2, v_norm_post2, v_w_in, v_hg_lb, v_hg_onorm, v_gla_w_gk, v_gla_b_gk, v_gla_onorm, v_w_br_hg, v_w_br_gla, v_w_out, v_w_ff_gate, v_w_ff_up, v_w_ff_down):
    seq, d = x.shape[1], x.shape[2]
    ctx_len = ctx.shape[1]
    assert seq % TM == 0 and ctx_len % TM == 0 and d == 2 * HW
    ax, ay, ac = lax.axis_index("x"), lax.axis_index("y"), lax.axis_index("c")
    chip = 2 * ax + ay
    dev = 2 * chip + ac
    c_arr = jnp.reshape(ac, (1,)).astype(jnp.int32)
    chip_arr = jnp.reshape(chip, (1,)).astype(jnp.int32)
    transposed = ("w_in", "w_ff_gate", "w_ff_up")
    view = lambda a, nm: a[0].T if nm in transposed else a[0]

    sems_in, lands_in, token_in0 = _blocks_start([_cast_into_blocks(chip_arr, view(w_in, "w_in"), "cast_w_in")],
                                                 "gather_w_in_start")

    nc = d // 128
    pad8 = lambda a: jnp.pad(a, ((0, -a.shape[0] % 8), (0, 0)))
    small1 = jnp.concatenate([c.reshape(nc, 128) + token_in0[0, 0], pad8(hg_lb.reshape(4, 128)),
                              gla_w_gk.reshape(2 * RANK, 128), pad8(gla_b_gk.reshape(2, 128))], axis=0)
    blocks = [_cast_into_blocks(chip_arr, view(w_, nm), "cast_" + nm) for w_, nm in (
        (w_br_hg, "w_br_hg"), (w_br_gla, "w_br_gla"), (w_out, "w_out"), (w_ff_gate, "w_ff_gate"),
        (w_ff_up, "w_ff_up"), (w_ff_down, "w_ff_down"))]
    got1 = _allgather8(small1, "gather_small_params", after=blocks)
    c_all = got1[:, :nc, :].reshape(N_DEV, d)
    per_chip = got1[0::2]
    lb_full = per_chip[:, nc:nc + 4, :].transpose(1, 0, 2).reshape(4, HW)
    wgk_full = per_chip[:, nc + 8:nc + 8 + 2 * RANK, :].transpose(1, 0, 2).reshape(2, RANK, HW)
    bgk_full = per_chip[:, nc + 8 + 2 * RANK:nc + 10 + 2 * RANK, :].transpose(1, 0, 2).reshape(2, HW)
    wgk_pad = [jnp.zeros((128, HW), F32).at[dd * RANK:(dd + 1) * RANK].set(wgk_full[dd]) for dd in range(2)]
    bgk = [bgk_full[dd:dd + 1] for dd in range(2)]

    n_mod_cols = w_mod.shape[2]
    cond = jnp.concatenate([c_all, pad8(c_ctx.reshape(1, d))], axis=0)
    b_cols = lax.dynamic_slice(b_mod, (0, chip * n_mod_cols), (1, n_mod_cols))
    lands_in = _blocks_wait(sems_in, lands_in, [got1], "gather_w_in_wait")
    fwd_sems, lands_in, fwd_token = _forward_start(lands_in, "gather_w_in_forward_start")
    mod_part = _mod_forward(cond + fwd_token[0, 0], w_mod[0], b_cols, "mod_forward")
    mod_got = _allgather8(mod_part, "gather_mod")
    mod_all = mod_got[0::2].transpose(1, 0, 2).reshape(16, N_CHIP * n_mod_cols)
    modx = pad8(lax.dynamic_slice(mod_all, (dev, 0), (1, N_MOD * d)).reshape(N_MOD, d))
    modc = pad8(mod_all[8].reshape(N_MOD, d))

    gathered_in = _forward_wait(fwd_sems, lands_in, [mod_got], "gather_w_in_forward_wait")
    sems, lands, token = _blocks_start(blocks, "gather_rest_start", after=[gathered_in[0]])
    w_in_r = gathered_in[0].reshape(-1, d)

    norms = jnp.concatenate([norm_pre1, norm_post1, norm_pre2, norm_post2, jnp.zeros((4, d), F32)], axis=0)
    onorms = jnp.zeros((8, d), F32).at[0, :HD].set(hg_onorm[0]).at[1, :HD].set(gla_onorm[0])
    gla_side = [(wgk_pad[dd], bgk[dd]) for dd in range(2)]
    modx = modx + token[0, 0]
    front = _sample_front(x[0], ctx[0], modc, modx, norm_pre1, lb_full, gla_side, w_in_r)
    lands = _blocks_wait(sems, lands, front["o_list"], "gather_rest_wait")
    gathered = _blocks_finish(lands[:3], "gather_mix_finish")
    wbh, wbg = _unblocked(gathered[0]), _unblocked(gathered[1])
    wout = gathered[2].reshape(d, d)
    ffn_sems, ffn_lands, ffn_token = _forward_start(lands[3:], "gather_ffn_forward_start")
    onorms = onorms + ffn_token[0, 0]

    def ffn_weights(after):
        got = _forward_wait(ffn_sems, ffn_lands, after, "gather_ffn_forward_wait")
        return tuple(g.reshape(-1, d) for g in got)

    dff = w_ff_down.shape[1] * N_CHIP
    groups = {"ffn": ["w_ff_gate", "w_ff_up", "w_ff_down"], "mix": ["w_br_hg", "w_br_gla", "w_out"], "in": ["w_in"]}
    row_sharded = {"w_out": d // N_CHIP, "w_ff_down": dff // N_CHIP, "w_ff_gate": dff // N_CHIP,
                   "w_ff_up": dff // N_CHIP, "w_in": w_in.shape[2]}
    in_flight, to_sibling = {}, {}

    small = {}

    def reduce_small(stats):
        small2 = jnp.concatenate([
            stats["stat_in"], stats["stat_mix"], stats["stat_ffn"],
            jnp.concatenate(stats["dlb"], axis=1), jnp.concatenate(stats["dbgk"], axis=1),
            jnp.concatenate([stats["dwgk"][0][0:RANK], stats["dwgk"][1][RANK:2 * RANK]], axis=1)], axis=0)
        assert small2.shape[0] == SMALL_ROWS
        got2 = _allgather8(small2, "gather_small_grads")
        total, dmod_all, g_b_mod, g_lb_full = _reduce_small(got2, lb_full, "reduce_small")
        dmod_cols = lax.dynamic_slice(dmod_all, (0, chip * n_mod_cols), (16, n_mod_cols))
        g_w_mod, cctx_part = _mod_backward(cond, w_mod[0], dmod_cols, "mod_backward")
        got3 = _allgather8(cctx_part, "gather_c_ctx_grad")
        g_c_ctx = _c_ctx_grad(got3, c_ctx.reshape(1, d), "c_ctx_grad")
        small.update(total=total, g_b_mod=g_b_mod, g_lb_full=g_lb_full, g_w_mod=g_w_mod, g_c_ctx=g_c_ctx)

    def reduce(group, grads):
        if group == "small":
            return reduce_small(grads)
        if group.startswith("push_"):
            return push(group[5:], grads)
        nms = groups[group]
        full = [g.reshape(N_CHIP, row_sharded[nm], d) if nm in row_sharded else _blocked(g, N_CHIP)
                for g, nm in zip(grads, nms)]
        sems_, full, lands_, token_ = _send_half_start(full, "grads_to_sibling_start_" + group)
        to_sibling[group] = (sems_, full, lands_)
        return token_[0, 0]

    def push(group, after):
        nms = groups[group]
        sems_, full, lands_ = to_sibling[group]
        if group == "in":
            after = list(after) + [small["g_c_ctx"], small["total"]]
        full, from_sibling = _send_half_wait(sems_, full, lands_, after, "grads_to_sibling_wait_" + group)
        pairs = [_pair_sum(c_arr, f, r_, "pair_sum_" + nm) for f, r_, nm in zip(full, from_sibling, nms)]
        after = [small["g_c_ctx"], small["total"]] if group == "in" else []
        sems_, pairs, lands_, token_ = _scatter_start(pairs, "grads_to_owner_start_" + group, after)
        in_flight[group] = (sems_, pairs, lands_, token_)
        return token_[0, 0]

    r = _sample_back(reduce, front, x[0], ctx[0], loss_target[0], modc, modx, norm_pre1, norms, onorms, lb_full,
                     gla_side, w_in_r, wbh, wbg, wout, ffn_weights)
    loss_part, grad_x, stat_in, stat_mix, stat_ffn = (r[k] for k in ("loss_part", "grad_x", "stat_in", "stat_mix",
                                                                     "stat_ffn"))
    (dlb_f, dlb_b), (dwgk_f, dwgk_b), (dbgk_f, dbgk_b) = r["dlb"], r["dwgk"], r["dbgk"]

    weights = dict(w_in=(w_in, m_w_in, v_w_in), w_br_hg=(w_br_hg, m_w_br_hg, v_w_br_hg),
                   w_br_gla=(w_br_gla, m_w_br_gla, v_w_br_gla), w_out=(w_out, m_w_out, v_w_out),
                   w_ff_gate=(w_ff_gate, m_w_ff_gate, v_w_ff_gate), w_ff_up=(w_ff_up, m_w_ff_up, v_w_ff_up),
                   w_ff_down=(w_ff_down, m_w_ff_down, v_w_ff_down))
    names = ["w_in", "w_br_hg", "w_br_gla", "w_out", "w_ff_gate", "w_ff_up", "w_ff_down"]
    big = {}

    swapping = {}

    def sum_and_swap(group, after):
        sems_, pairs, lands_, _ = in_flight[group]
        pairs, lands_ = _scatter_wait(sems_, pairs, lands_, after, "grads_to_owner_wait_" + group)
        own_half = [_sum_owner(chip_arr, pr, g, "chip_sum_" + nm) for pr, g, nm in zip(pairs, lands_, groups[group])]
        swapping[group] = _swap_start(own_half, "halves_to_sibling_start_" + group)
        return own_half[-1]

    def update(group, after):
        sems_, own_half, lands_ = swapping[group]
        own_half, other_half = _swap_wait(sems_, own_half, lands_, after, "halves_to_sibling_wait_" + group)
        done = []
        for nm, own, oth in zip(groups[group], own_half, other_half):
            w_, m_, v_ = (view(a, nm) for a in weights[nm])
            res = _adamw_halves(c_arr, own, oth, w_, m_, v_, "adamw_" + nm)
            big[nm] = [r_.T[None] if nm in transposed else r_[None] for r_ in res]
            done.append(res[1])
        return done

    token_in = in_flight["in"][3]
    summed_ffn = sum_and_swap("ffn", [token_in])
    summed_mix = sum_and_swap("mix", [summed_ffn])

    total, g_b_mod, g_lb_full, g_w_mod, g_c_ctx = (small[k] for k in ("total", "g_b_mod", "g_lb_full", "g_w_mod",
                                                                      "g_c_ctx"))
    g_pre1, g_post1, g_pre2, g_post2 = (total[r_:r_ + 1] for r_ in (ROW_PRE1, ROW_POST1, ROW_PRE2, ROW_POST2))
    g_hg_on, g_gla_on = total[ROW_ONORM:ROW_ONORM + 1, 0:HD], total[ROW_ONORM:ROW_ONORM + 1, HD:2 * HD]
    n_lb = hg_lb.shape[2]
    g_hg_lb = lax.dynamic_slice(g_lb_full, (0, chip * n_lb), (4, n_lb))
    g_bgk = lax.dynamic_slice(total[ROW_BGK:ROW_BGK + 1].reshape(2, HW), (0, chip * n_lb), (2, n_lb))
    g_wgk_full = total[ROW_WGK:ROW_WGK + RANK].reshape(RANK, 2, HW).transpose(1, 0, 2).reshape(2 * RANK, HW)
    g_wgk = lax.dynamic_slice(g_wgk_full, (0, chip * n_lb), (2 * RANK, n_lb))

    small_items = [
        (g_c_ctx, c_ctx.reshape(1, d), m_c_ctx.reshape(1, d), v_c_ctx.reshape(1, d)),
        (g_b_mod, b_mod, m_b_mod, v_b_mod),
        (g_pre1, norm_pre1, m_norm_pre1, v_norm_pre1),
        (g_post1, norm_post1, m_norm_post1, v_norm_post1),
        (g_pre2, norm_pre2, m_norm_pre2, v_norm_pre2),
        (g_post2, norm_post2, m_norm_post2, v_norm_post2),
        (g_hg_lb, hg_lb.reshape(4, n_lb), m_hg_lb.reshape(4, n_lb), v_hg_lb.reshape(4, n_lb)),
        (g_hg_on, hg_onorm, m_hg_onorm, v_hg_onorm),
        (g_wgk, gla_w_gk.reshape(2 * RANK, n_lb), m_gla_w_gk.reshape(2 * RANK, n_lb), v_gla_w_gk.reshape(2 * RANK, n_lb)),
        (g_bgk, gla_b_gk.reshape(2, n_lb), m_gla_b_gk.reshape(2, n_lb), v_gla_b_gk.reshape(2, n_lb)),
        (g_gla_on, gla_onorm, m_gla_onorm, v_gla_onorm),
    ]
    small_res = _adamw_whole(small_items, "adamw_small")
    mod_res = _adamw_tiled(g_w_mod, w_mod[0], m_w_mod[0], v_w_mod[0], "adamw_w_mod")
    done_ffn = update("ffn", [summed_mix, mod_res[0], small_res[0][0]])
    done_mix = update("mix", done_ffn)
    update("in", [sum_and_swap("in", done_mix)])

    loss = total[ROW_LOSS, 0]

    shapes = dict(c_ctx=c_ctx.shape, b_mod=b_mod.shape, norm_pre1=norm_pre1.shape, norm_post1=norm_post1.shape,
                  norm_pre2=norm_pre2.shape, norm_post2=norm_post2.shape, hg_lb=hg_lb.shape, hg_onorm=hg_onorm.shape,
                  gla_w_gk=gla_w_gk.shape, gla_b_gk=gla_b_gk.shape, gla_onorm=gla_onorm.shape)
    small_names = ["c_ctx", "b_mod", "norm_pre1", "norm_post1", "norm_pre2", "norm_post2", "hg_lb", "hg_onorm",
                   "gla_w_gk", "gla_b_gk", "gla_onorm"]
    grads, deltas, new_m, new_v = {}, {}, {}, {}
    for nm, item, res in zip(small_names, small_items, small_res):
        grads[nm] = item[0].reshape(shapes[nm])
        deltas[nm], new_m[nm], new_v[nm] = (r.reshape(shapes[nm]) for r in res)
    grads["w_mod"] = g_w_mod[None]
    deltas["w_mod"], new_m["w_mod"], new_v["w_mod"] = (r[None] for r in mod_res)
    for nm in names:
        grads[nm], deltas[nm], new_m[nm], new_v[nm] = big[nm]
    order = ["c_ctx", "w_mod", "b_mod", "norm_pre1", "norm_post1", "norm_pre2", "norm_post2", "w_in", "hg_lb",
             "hg_onorm", "gla_w_gk", "gla_b_gk", "gla_onorm", "w_br_hg", "w_br_gla", "w_out", "w_ff_gate", "w_ff_up",
             "w_ff_down"]
    return (loss, grad_x[None], *[grads[n] for n in order], *[deltas[n] for n in order],
            *[new_m[n] for n in order], *[new_v[n] for n in order])


def _weight_grad_cols(xs, dy, n_cols, name, tn=512):
    rows = dy.shape[0]
    k = tk = xs.shape[1]

    return pl.pallas_call(
        functools.partial(_transposed_lhs_matmul), name=name, grid=(k // tk, n_cols // tn),
        out_shape=jax.ShapeDtypeStruct((k, n_cols), F32),
        in_specs=[pl.BlockSpec((rows, tk), lambda i, j: (0, i)), pl.BlockSpec((rows, tn), lambda i, j: (0, j))],
        out_specs=pl.BlockSpec((tk, tn), lambda i, j: (i, j)),
        scratch_shapes=[pltpu.VMEM((tk, rows), BF16)],
        compiler_params=_cparams(dimension_semantics=("parallel", "arbitrary")),
    )(xs, dy)
```

```python
import functools

import jax
import jax.numpy as jnp
from jax import lax
from jax.experimental import pallas as pl
from jax.experimental.pallas import tpu as pltpu

F32 = jnp.float32
BF16 = jnp.bfloat16
MESH = pl.DeviceIdType.MESH

EPS = 1e-6
CHUNK = 64
SUB = 16
NSUB = CHUNK // SUB
NH = 4
HD = 128
HW = NH * HD
RANK = 16
GATE_NORM = 16.0
N_MOD = 6
TM = 256
BWD_LANES = 8
N_DEV = 8
N_CHIP = 4
VMEM_LIMIT = 56 * 1024 * 1024

ADAM_LR = 0.001
ADAM_B1 = 0.9
ADAM_B2 = 0.999
ADAM_EPS = 1e-08
ADAM_WD = 0.01
ADAM_STEP = 10

VMEM_SPEC = pl.BlockSpec(memory_space=pltpu.VMEM)
ANY_SPEC = pl.BlockSpec(memory_space=pl.ANY)
HBM_SPEC = pl.BlockSpec(memory_space=pltpu.HBM)
SEM_SPEC = pl.BlockSpec(memory_space=pltpu.SEMAPHORE)
EFFECT = pltpu.SideEffectType.DATAFLOW_SIDE_EFFECTING


def _cparams(**kw):
    return pltpu.CompilerParams(vmem_limit_bytes=VMEM_LIMIT, **kw)


def _dot(a, b):
    return jnp.dot(a.astype(BF16), b.astype(BF16), preferred_element_type=F32)


def _dot_nt(a, b):
    return lax.dot_general(a.astype(BF16), b.astype(BF16), (((1,), (1,)), ((), ())), preferred_element_type=F32)


def _dot_tn(a, b):
    return lax.dot_general(a.astype(BF16), b.astype(BF16), (((0,), (0,)), ((), ())), preferred_element_type=F32)


def _sigmoid(x):
    return 1.0 / (1.0 + jnp.exp(-x))


def _silu(x):
    return x * _sigmoid(x)


def _dsilu(x):
    s = _sigmoid(x)
    return s * (1.0 + x * (1.0 - s))


def _log_sigmoid(x):
    return jnp.minimum(x, 0.0) - jnp.log(1.0 + jnp.exp(-jnp.abs(x)))


def _colsum(a):
    return jnp.sum(a, axis=0, keepdims=True)


def _rms(a):
    r = lax.rsqrt(jnp.mean(a * a, axis=-1, keepdims=True) + EPS)
    return a * r, r


def _rms_bwd(dn, n, r):
    return r * (dn - n * jnp.mean(dn * n, axis=-1, keepdims=True))


def _place():
    x, y, c = lax.axis_index("x"), lax.axis_index("y"), lax.axis_index("c")
    chips = [(1 - x, y), (x, 1 - y), (1 - x, 1 - y)]
    return x, y, c, chips


def _allgather8(v, name, after=()):
    rows, cols = v.shape
    n_after = len(after)

    def body(x_ref, *rest):
        out_ref, send_sems, recv_sems, local_sem = rest[n_after:]
        x, y, c, chips = _place()
        me, sibling = (x, y, c), (x, y, 1 - c)

        def blk(px, py, pc):
            return out_ref.at[4 * px + 2 * py + pc]

        def copy(k, block, to, src=None):
            return pltpu.make_async_remote_copy(
                src_ref=blk(*block) if src is None else src, dst_ref=blk(*block),
                send_sem=send_sems.at[k], recv_sem=recv_sems.at[k], device_id=to, device_id_type=MESH)

        mine = pltpu.make_async_copy(x_ref, blk(*me), local_sem)
        mine.start()
        first = [copy(0, me, sibling, src=x_ref)]
        first += [copy(1 + j, me, (*chip, c), src=x_ref) for j, chip in enumerate(chips)]
        for cp in first:
            cp.start()
        passed = [copy(4 + j, (*chip, c), sibling) for j, chip in enumerate(chips)]
        for j, chip in enumerate(chips):
            copy(1 + j, (*chip, c), me).wait_recv()
            passed[j].start()
        copy(0, sibling, me).wait_recv()
        for j, chip in enumerate(chips):
            copy(4 + j, (*chip, 1 - c), me).wait_recv()
        for cp in first + passed:
            cp.wait_send()
        mine.wait()

    return pl.pallas_call(
        body, name=name,
        out_shape=jax.ShapeDtypeStruct((N_DEV, rows, cols), v.dtype),
        in_specs=[VMEM_SPEC] + [ANY_SPEC] * n_after, out_specs=VMEM_SPEC,
        scratch_shapes=[pltpu.SemaphoreType.DMA((7,)), pltpu.SemaphoreType.DMA((7,)), pltpu.SemaphoreType.DMA],
    )(v, *after)


def _cast_into_blocks(chip_arr, w, name):
    rows, cols = w.shape
    tr = _row_tile(rows, 16, 256)

    def body(chip_ref, w_ref, o_ref):
        o_ref[0] = w_ref[...].astype(BF16)

    return pl.pallas_call(
        body, name=name,
        grid_spec=pltpu.PrefetchScalarGridSpec(
            num_scalar_prefetch=1, grid=(rows // tr,),
            in_specs=[pl.BlockSpec((tr, cols), lambda i, chip_ref: (i, 0))],
            out_specs=pl.BlockSpec((1, tr, cols), lambda i, chip_ref: (chip_ref[0], i, 0))),
        out_shape=jax.ShapeDtypeStruct((N_CHIP, rows, cols), BF16),
        compiler_params=_cparams(dimension_semantics=("parallel",)),
    )(chip_arr, w)


def _halved_by_rows(shape):
    return (shape[1] // 2) % 16 == 0


def _half_of(ref, pc, block=None):
    lead = slice(None) if block is None else block
    if _halved_by_rows(ref.shape):
        h = ref.shape[1] // 2
        return ref.at[lead, pl.ds(pl.multiple_of(pc * h, 16), h), :]
    h = ref.shape[2] // 2
    return ref.at[lead, :, pl.ds(pl.multiple_of(pc * h, 128), h)]


def _half_shape(shape):
    return (shape[0], shape[1] // 2, shape[2]) if _halved_by_rows(shape) else (shape[0], shape[1], shape[2] // 2)


def _half_rows(ref, chip_id, pc):
    return _half_of(ref, pc, chip_id)


def _hbm(a):
    return pltpu.with_memory_space_constraint(a, pltpu.HBM)


def _blocks_start(lands, name, after=()):
    n = len(lands)
    n_sem = 3 * n
    first = n + len(after)

    def body(*refs):
        lnd = refs[:n]
        send_sems, recv_sems = refs[first:first + n_sem], refs[first + n_sem:first + 2 * n_sem]
        token = refs[-1]
        x, y, c, chips = _place()
        me_chip = 2 * x + y
        for k in range(n):
            for j, chip in enumerate(chips):
                pltpu.make_async_remote_copy(
                    src_ref=_half_rows(lnd[k], me_chip, c), dst_ref=_half_rows(lnd[k], me_chip, c),
                    send_sem=send_sems[3 * k + j], recv_sem=recv_sems[3 * k + j],
                    device_id=(*chip, c), device_id_type=MESH).start()
        token[...] = jnp.zeros_like(token)

    out = pl.pallas_call(
        body, name=name,
        out_shape=(*[pltpu.SemaphoreType.DMA(())] * (2 * n_sem),
                   *[pltpu.HBM(l.shape, l.dtype) for l in lands],
                   jax.ShapeDtypeStruct((8, 128), F32)),
        in_specs=[HBM_SPEC] * n + [ANY_SPEC] * len(after),
        out_specs=(*[SEM_SPEC] * (2 * n_sem), *[HBM_SPEC] * n, VMEM_SPEC),
        input_output_aliases={i: 2 * n_sem + i for i in range(n)},
        compiler_params=pltpu.CompilerParams(has_side_effects=EFFECT),
    )(*[_hbm(l) for l in lands], *after)
    return list(out[:2 * n_sem]), list(out[2 * n_sem:2 * n_sem + n]), out[-1]


def _blocks_wait(sems, lands, after, name):
    n = len(lands)
    n_sem = 3 * n

    def body(*refs):
        lnd = refs[:n]
        s_sems, r_sems = refs[n:n + n_sem], refs[n + n_sem:n + 2 * n_sem]
        x, y, c, chips = _place()
        me_chip = 2 * x + y
        for k in range(n):
            for j, (px, py) in enumerate(chips):
                cp = pltpu.make_async_remote_copy(
                    src_ref=_half_rows(lnd[k], me_chip, c), dst_ref=_half_rows(lnd[k], 2 * px + py, c),
                    send_sem=s_sems[3 * k + j], recv_sem=r_sems[3 * k + j],
                    device_id=(px, py, c), device_id_type=MESH)
                cp.wait_send()
                cp.wait_recv()

    out = pl.pallas_call(
        body, name=name,
        out_shape=tuple(pltpu.HBM(l.shape, l.dtype) for l in lands),
        in_specs=[HBM_SPEC] * n + [SEM_SPEC] * (2 * n_sem) + [ANY_SPEC] * len(after),
        out_specs=[HBM_SPEC] * n,
        input_output_aliases={i: i for i in range(n)},
        compiler_params=pltpu.CompilerParams(has_side_effects=EFFECT),
    )(*lands, *sems, *after)
    return list(out)


def _forward_start(lands, name):
    n = len(lands)
    n_sem = 3 * n

    def body(*refs):
        lnd = refs[:n]
        send_sems, recv_sems = refs[n:n + n_sem], refs[n + n_sem:n + 2 * n_sem]
        x, y, c, chips = _place()
        for k in range(n):
            for j, (px, py) in enumerate(chips):
                pltpu.make_async_remote_copy(
                    src_ref=_half_rows(lnd[k], 2 * px + py, c), dst_ref=_half_rows(lnd[k], 2 * px + py, c),
                    send_sem=send_sems[3 * k + j], recv_sem=recv_sems[3 * k + j],
                    device_id=(x, y, 1 - c), device_id_type=MESH).start()
        refs[-1][...] = jnp.zeros_like(refs[-1])

    out = pl.pallas_call(
        body, name=name,
        out_shape=(*[pltpu.SemaphoreType.DMA(())] * (2 * n_sem), *[pltpu.HBM(l.shape, l.dtype) for l in lands],
                   jax.ShapeDtypeStruct((8, 128), F32)),
        in_specs=[HBM_SPEC] * n,
        out_specs=(*[SEM_SPEC] * (2 * n_sem), *[HBM_SPEC] * n, VMEM_SPEC),
        input_output_aliases={i: 2 * n_sem + i for i in range(n)},
        compiler_params=pltpu.CompilerParams(has_side_effects=EFFECT),
    )(*[_hbm(l) for l in lands])
    return list(out[:2 * n_sem]), list(out[2 * n_sem:2 * n_sem + n]), out[-1]


def _forward_wait(sems, lands, after, name):
    n = len(lands)
    n_sem = 3 * n

    def body(*refs):
        lnd = refs[:n]
        s_sems, r_sems = refs[n:n + n_sem], refs[n + n_sem:n + 2 * n_sem]
        x, y, c, chips = _place()
        for k in range(n):
            for j, (px, py) in enumerate(chips):
                cp = pltpu.make_async_remote_copy(
                    src_ref=_half_rows(lnd[k], 2 * px + py, c), dst_ref=_half_rows(lnd[k], 2 * px + py, 1 - c),
                    send_sem=s_sems[3 * k + j], recv_sem=r_sems[3 * k + j],
                    device_id=(x, y, 1 - c), device_id_type=MESH)
                cp.wait_send()
                cp.wait_recv()

    out = pl.pallas_call(
        body, name=name,
        out_shape=tuple(pltpu.HBM(l.shape, l.dtype) for l in lands),
        in_specs=[HBM_SPEC] * n + [SEM_SPEC] * (2 * n_sem) + [ANY_SPEC] * len(after),
        out_specs=[HBM_SPEC] * n,
        input_output_aliases={i: i for i in range(n)},
        compiler_params=pltpu.CompilerParams(has_side_effects=EFFECT),
    )(*lands, *sems, *after)
    return list(out)


def _blocks_finish(lands, name):
    n = len(lands)

    def body(*refs):
        lnd = refs[n:2 * n]
        send_sems, recv_sems = refs[2 * n:]
        x, y, c, chips = _place()
        sibling = (x, y, 1 - c)

        def copy(k, j, chip_id, pc):
            return pltpu.make_async_remote_copy(
                src_ref=_half_rows(lnd[k], chip_id, pc), dst_ref=_half_rows(lnd[k], chip_id, pc),
                send_sem=send_sems.at[k, j], recv_sem=recv_sems.at[k, j], device_id=sibling, device_id_type=MESH)

        started = []
        for k in range(n):
            for j, (px, py) in enumerate(chips):
                cp = copy(k, j, 2 * px + py, c)
                cp.start()
                started.append(cp)
        for k in range(n):
            for j, (px, py) in enumerate(chips):
                copy(k, j, 2 * px + py, 1 - c).wait_recv()
        for cp in started:
            cp.wait_send()

    out = pl.pallas_call(
        body, name=name,
        out_shape=[jax.ShapeDtypeStruct(l.shape, l.dtype) for l in lands],
        in_specs=[ANY_SPEC] * n, out_specs=[ANY_SPEC] * n,
        input_output_aliases={i: i for i in range(n)},
        scratch_shapes=[pltpu.SemaphoreType.DMA((n, 3)), pltpu.SemaphoreType.DMA((n, 3))],
    )(*lands)
    return list(out)


def _send_half_start(arrs, name):
    n = len(arrs)

    def body(*refs):
        ins, lnd = refs[:n], refs[n:2 * n]
        send_sems, recv_sems = refs[2 * n:3 * n], refs[3 * n:4 * n]
        token = refs[-1]
        x, y, c, _ = _place()
        for k in range(n):
            pltpu.make_async_remote_copy(
                src_ref=_half_of(ins[k], 1 - c), dst_ref=lnd[k], send_sem=send_sems[k], recv_sem=recv_sems[k],
                device_id=(x, y, 1 - c), device_id_type=MESH).start()
        token[...] = jnp.zeros_like(token)

    lands = [_hbm(lax.empty(_half_shape(a.shape), a.dtype)) for a in arrs]
    out = pl.pallas_call(
        body, name=name,
        out_shape=(*[pltpu.SemaphoreType.DMA(())] * (2 * n), *[pltpu.HBM(a.shape, a.dtype) for a in arrs],
                   *[pltpu.HBM(l.shape, l.dtype) for l in lands], jax.ShapeDtypeStruct((8, 128), F32)),
        in_specs=[HBM_SPEC] * (2 * n),
        out_specs=(*[SEM_SPEC] * (2 * n), *[HBM_SPEC] * (2 * n), VMEM_SPEC),
        input_output_aliases={i: 2 * n + i for i in range(2 * n)},
        compiler_params=pltpu.CompilerParams(has_side_effects=EFFECT),
    )(*[_hbm(a) for a in arrs], *lands)
    return list(out[:2 * n]), list(out[2 * n:3 * n]), list(out[3 * n:4 * n]), out[-1]


def _send_half_wait(sems, arrs, lands, after, name):
    n = len(arrs)

    def body(*refs):
        ins, lnd = refs[:n], refs[n:2 * n]
        s_sems, r_sems = refs[2 * n:3 * n], refs[3 * n:4 * n]
        x, y, c, _ = _place()
        for k in range(n):
            cp = pltpu.make_async_remote_copy(
                src_ref=_half_of(ins[k], 1 - c), dst_ref=lnd[k], send_sem=s_sems[k], recv_sem=r_sems[k],
                device_id=(x, y, 1 - c), device_id_type=MESH)
            cp.wait_send()
            cp.wait_recv()

    out = pl.pallas_call(
        body, name=name,
        out_shape=tuple(pltpu.HBM(a.shape, a.dtype) for a in list(arrs) + list(lands)),
        in_specs=[HBM_SPEC] * (2 * n) + [SEM_SPEC] * (2 * n) + [ANY_SPEC] * len(after),
        out_specs=[HBM_SPEC] * (2 * n),
        input_output_aliases={i: i for i in range(2 * n)},
        compiler_params=pltpu.CompilerParams(has_side_effects=EFFECT),
    )(*arrs, *lands, *sems, *after)
    return list(out[:n]), list(out[n:])


def _scatter_start(arrs, name, after=()):
    n = len(arrs)
    n_sem = 3 * n
    first = 2 * n + len(after)

    def body(*refs):
        ins, lnd = refs[:n], refs[n:2 * n]
        send_sems, recv_sems = refs[first:first + n_sem], refs[first + n_sem:first + 2 * n_sem]
        token = refs[-1]
        x, y, c, chips = _place()
        me_chip = 2 * x + y
        for k in range(n):
            for j, (px, py) in enumerate(chips):
                pltpu.make_async_remote_copy(
                    src_ref=ins[k].at[2 * px + py], dst_ref=lnd[k].at[me_chip],
                    send_sem=send_sems[3 * k + j], recv_sem=recv_sems[3 * k + j],
                    device_id=(px, py, c), device_id_type=MESH).start()
        token[...] = jnp.zeros_like(token)

    lands = [_hbm(lax.empty(a.shape, a.dtype)) for a in arrs]
    out = pl.pallas_call(
        body, name=name,
        out_shape=(*[pltpu.SemaphoreType.DMA(())] * (2 * n_sem),
                   *[pltpu.HBM(a.shape, a.dtype) for a in arrs], *[pltpu.HBM(a.shape, a.dtype) for a in arrs],
                   jax.ShapeDtypeStruct((8, 128), F32)),
        in_specs=[HBM_SPEC] * (2 * n) + [ANY_SPEC] * len(after),
        out_specs=(*[SEM_SPEC] * (2 * n_sem), *[HBM_SPEC] * (2 * n), VMEM_SPEC),
        input_output_aliases={i: 2 * n_sem + i for i in range(2 * n)},
        compiler_params=pltpu.CompilerParams(has_side_effects=EFFECT),
    )(*[_hbm(a) for a in arrs], *lands, *after)
    base = 2 * n_sem
    return list(out[:base]), list(out[base:base + n]), list(out[base + n:base + 2 * n]), out[-1]


def _scatter_wait(sems, arrs, lands, after, name):
    n = len(arrs)
    n_sem = 3 * n

    def body(*refs):
        ins, lnd = refs[:n], refs[n:2 * n]
        s_sems, r_sems = refs[2 * n:2 * n + n_sem], refs[2 * n + n_sem:2 * n + 2 * n_sem]
        x, y, c, chips = _place()
        for k in range(n):
            for j, (px, py) in enumerate(chips):
                cp = pltpu.make_async_remote_copy(
                    src_ref=ins[k].at[2 * px + py], dst_ref=lnd[k].at[2 * px + py],
                    send_sem=s_sems[3 * k + j], recv_sem=r_sems[3 * k + j],
                    device_id=(px, py, c), device_id_type=MESH)
                cp.wait_send()
                cp.wait_recv()

    out = pl.pallas_call(
        body, name=name,
        out_shape=tuple(pltpu.HBM(a.shape, a.dtype) for a in list(arrs) + list(lands)),
        in_specs=[HBM_SPEC] * (2 * n) + [SEM_SPEC] * (2 * n_sem) + [ANY_SPEC] * len(after),
        out_specs=[HBM_SPEC] * (2 * n),
        input_output_aliases={i: i for i in range(2 * n)},
        compiler_params=pltpu.CompilerParams(has_side_effects=EFFECT),
    )(*arrs, *lands, *sems, *after)
    return list(out[:n]), list(out[n:])


def _sum_owner(chip_arr, pairs, got, name):
    nb, h, cols = got.shape
    tr = _row_tile(h, 16, 256)

    def body(chip_ref, own_ref, a_ref, b_ref, c_ref, o_ref):
        o_ref[...] = ((own_ref[0].astype(F32) + a_ref[0].astype(F32)) + b_ref[0].astype(F32)) + c_ref[0].astype(F32)

    def slot(off):
        return pl.BlockSpec((1, tr, cols), lambda i, chip_ref: ((chip_ref[0] + off) % N_CHIP, i, 0))

    return pl.pallas_call(
        body, name=name,
        grid_spec=pltpu.PrefetchScalarGridSpec(
            num_scalar_prefetch=1, grid=(h // tr,),
            in_specs=[slot(0), slot(1), slot(2), slot(3)],
            out_specs=pl.BlockSpec((tr, cols), lambda i, chip_ref: (i, 0))),
        out_shape=jax.ShapeDtypeStruct((h, cols), F32),
        compiler_params=_cparams(dimension_semantics=("parallel",)),
    )(chip_arr, pairs, got, got, got)


def _swap_start(arrs, name, after=()):
    n = len(arrs)
    first = 2 * n + len(after)

    def body(*refs):
        ins, lnd = refs[:n], refs[n:2 * n]
        send_sems, recv_sems = refs[first:first + n], refs[first + n:first + 2 * n]
        x, y, c, _ = _place()
        for k in range(n):
            pltpu.make_async_remote_copy(
                src_ref=ins[k], dst_ref=lnd[k], send_sem=send_sems[k], recv_sem=recv_sems[k],
                device_id=(x, y, 1 - c), device_id_type=MESH).start()

    lands = [_hbm(lax.empty(a.shape, a.dtype)) for a in arrs]
    out = pl.pallas_call(
        body, name=name,
        out_shape=(*[pltpu.SemaphoreType.DMA(())] * (2 * n), *[pltpu.HBM(a.shape, a.dtype) for a in arrs],
                   *[pltpu.HBM(a.shape, a.dtype) for a in arrs]),
        in_specs=[HBM_SPEC] * (2 * n) + [ANY_SPEC] * len(after),
        out_specs=(*[SEM_SPEC] * (2 * n), *[HBM_SPEC] * (2 * n)),
        input_output_aliases={i: 2 * n + i for i in range(2 * n)},
        compiler_params=pltpu.CompilerParams(has_side_effects=EFFECT),
    )(*[_hbm(a) for a in arrs], *lands, *after)
    return list(out[:2 * n]), list(out[2 * n:3 * n]), list(out[3 * n:4 * n])


def _swap_wait(sems, arrs, lands, after, name):
    n = len(arrs)

    def body(*refs):
        ins, lnd = refs[:n], refs[n:2 * n]
        s_sems, r_sems = refs[2 * n:3 * n], refs[3 * n:4 * n]
        x, y, c, _ = _place()
        for k in range(n):
            cp = pltpu.make_async_remote_copy(
                src_ref=ins[k], dst_ref=lnd[k], send_sem=s_sems[k], recv_sem=r_sems[k],
                device_id=(x, y, 1 - c), device_id_type=MESH)
            cp.wait_send()
            cp.wait_recv()

    out = pl.pallas_call(
        body, name=name,
        out_shape=tuple(pltpu.HBM(a.shape, a.dtype) for a in list(arrs) + list(lands)),
        in_specs=[HBM_SPEC] * (2 * n) + [SEM_SPEC] * (2 * n) + [ANY_SPEC] * len(after),
        out_specs=[HBM_SPEC] * (2 * n),
        input_output_aliases={i: i for i in range(2 * n)},
        compiler_params=pltpu.CompilerParams(has_side_effects=EFFECT),
    )(*arrs, *lands, *sems, *after)
    return list(out[:n]), list(out[n:])


def _row_tile(h, mult=8, cap=128):
    for t in range(cap - cap % mult, mult - 1, -mult):
        if h % t == 0:
            return t
    if mult > 8:
        return _row_tile(h, 8, cap)
    raise ValueError(h)


def _pair_sum(c_arr, full, recv, name):
    nb, rows, cols = full.shape

    def body(c_ref, f_ref, r_ref, o_ref):
        o_ref[...] = (f_ref[...] + r_ref[...]).astype(BF16)

    if _halved_by_rows(full.shape):
        h = rows // 2
        tr = _row_tile(h, 16, 256)
        steps = h // tr
        own = pl.BlockSpec((1, tr, cols), lambda b, i, c_ref: (b, c_ref[0] * steps + i, 0))
        half = pl.BlockSpec((1, tr, cols), lambda b, i, c_ref: (b, i, 0))
    else:
        steps = 1
        own = pl.BlockSpec((1, rows, cols // 2), lambda b, i, c_ref: (b, 0, c_ref[0]))
        half = pl.BlockSpec((1, rows, cols // 2), lambda b, i, c_ref: (b, 0, 0))
    return pl.pallas_call(
        body, name=name,
        grid_spec=pltpu.PrefetchScalarGridSpec(
            num_scalar_prefetch=1, grid=(nb, steps), in_specs=[own, half], out_specs=half),
        out_shape=jax.ShapeDtypeStruct(_half_shape(full.shape), BF16),
        compiler_params=_cparams(dimension_semantics=("parallel", "parallel")),
    )(c_arr, full, recv)


def _adam_math(g, w, m, v):
    m1 = ADAM_B1 * m + (1.0 - ADAM_B1) * g
    v1 = ADAM_B2 * v + (1.0 - ADAM_B2) * (g * g)
    m_hat = m1 / (1.0 - ADAM_B1 ** ADAM_STEP)
    v_hat = v1 / (1.0 - ADAM_B2 ** ADAM_STEP)
    delta = -ADAM_LR * (m_hat / (jnp.sqrt(v_hat) + ADAM_EPS) + ADAM_WD * w)
    return delta, m1, v1


def _adamw_halves(c_arr, own, other, w, m, v, name):
    rows, cols = w.shape
    by_rows = own.shape[1] == cols

    def body(c_ref, own_ref, oth_ref, w_ref, m_ref, v_ref, g_out, d_out, m_out, v_out):
        if by_rows:
            g = jnp.where(pl.program_id(0) == c_ref[0], own_ref[...], oth_ref[...])
        else:
            own_, oth_ = own_ref[...], oth_ref[...]
            g = jnp.where(c_ref[0] == 0, jnp.concatenate([own_, oth_], axis=1), jnp.concatenate([oth_, own_], axis=1))
        d, m1, v1 = _adam_math(g, w_ref[...], m_ref[...], v_ref[...])
        g_out[...] = g
        d_out[...] = d
        m_out[...] = m1
        v_out[...] = v1

    if by_rows:
        h = rows // 2
        tr = _row_tile(h)
        steps = h // tr
        grid = (2, steps)
        half_spec = pl.BlockSpec((tr, cols), lambda p, i, c_ref: (i, 0))
        full_spec = pl.BlockSpec((tr, cols), lambda p, i, c_ref: (p * steps + i, 0))
    else:
        tr = _row_tile(rows)
        grid = (1, rows // tr)
        half_spec = pl.BlockSpec((tr, cols // 2), lambda p, i, c_ref: (i, 0))
        full_spec = pl.BlockSpec((tr, cols), lambda p, i, c_ref: (i, 0))
    return pl.pallas_call(
        body, name=name,
        grid_spec=pltpu.PrefetchScalarGridSpec(
            num_scalar_prefetch=1, grid=grid,
            in_specs=[half_spec, half_spec, full_spec, full_spec, full_spec],
            out_specs=[full_spec] * 4),
        out_shape=[jax.ShapeDtypeStruct(w.shape, F32)] * 4,
        compiler_params=_cparams(dimension_semantics=("parallel", "parallel")),
    )(c_arr, own, other, w, m, v)


def _adamw_whole(items, name):
    n = len(items)

    def body(*refs):
        ins, outs = refs[:4 * n], refs[4 * n:]
        for k in range(n):
            g, w, m, v = (r[...] for r in ins[4 * k:4 * k + 4])
            d, m1, v1 = _adam_math(g, w, m, v)
            outs[3 * k][...] = d
            outs[3 * k + 1][...] = m1
            outs[3 * k + 2][...] = v1

    flat = [a for it in items for a in it]
    shapes = [jax.ShapeDtypeStruct(it[1].shape, F32) for it in items for _ in range(3)]
    out = pl.pallas_call(
        body, name=name, out_shape=shapes,
        in_specs=[VMEM_SPEC] * (4 * n), out_specs=[VMEM_SPEC] * (3 * n),
        compiler_params=_cparams(),
    )(*flat)
    return [tuple(out[3 * k:3 * k + 3]) for k in range(n)]


def _adamw_tiled(g, w, m, v, name):
    rows, cols = w.shape
    tr = _row_tile(rows)

    def body(g_ref, w_ref, m_ref, v_ref, d_out, m_out, v_out):
        d, m1, v1 = _adam_math(g_ref[...], w_ref[...], m_ref[...], v_ref[...])
        d_out[...] = d
        m_out[...] = m1
        v_out[...] = v1

    spec = pl.BlockSpec((tr, cols), lambda i: (i, 0))
    return pl.pallas_call(
        body, name=name, grid=(rows // tr,),
        out_shape=[jax.ShapeDtypeStruct(w.shape, F32)] * 3,
        in_specs=[spec] * 4, out_specs=[spec] * 3,
        compiler_params=_cparams(dimension_semantics=("parallel",)),
    )(g, w, m, v)


def _mod_forward(cond, w_mod, b_mod_cols, name):
    def body(c_ref, w_ref, b_ref, o_ref):
        o_ref[...] = _dot(_silu(c_ref[...]), w_ref[...]) + b_ref[...]

    return pl.pallas_call(
        body, name=name, out_shape=jax.ShapeDtypeStruct((cond.shape[0], w_mod.shape[1]), F32),
        in_specs=[VMEM_SPEC] * 3, out_specs=VMEM_SPEC, compiler_params=_cparams(),
    )(cond, w_mod, b_mod_cols)


def _mod_backward(cond, w_mod, dmod_cols, name):
    def body(c_ref, w_ref, d_ref, gw_ref, gc_ref):
        s = _silu(c_ref[...])
        d = d_ref[...]
        gw_ref[...] = _dot_tn(s, d)
        gc_ref[...] = _dot_nt(d[8:16, :], w_ref[...])

    return pl.pallas_call(
        body, name=name,
        out_shape=[jax.ShapeDtypeStruct(w_mod.shape, F32), jax.ShapeDtypeStruct((8, w_mod.shape[0]), F32)],
        in_specs=[VMEM_SPEC] * 3, out_specs=[VMEM_SPEC] * 2, compiler_params=_cparams(),
    )(cond, w_mod, dmod_cols)


def _col_chunks(width, step=512):
    return [(s, min(step, width - s)) for s in range(0, width, step)]


def _w_in_row(p_off):
    if p_off < 9 * HW:
        return p_off
    return 9 * HW if p_off == OFF_LR else p_off + 2 * RANK


def _in_projection(ctx0, x0, modc, modx, pre1, w_t, n_ctx_tiles, name):
    d = x0.shape[1]
    rows = ctx0.shape[0] + x0.shape[0]
    width = P_WIDTH

    def body(ctx_ref, x_ref, modc_ref, modx_ref, pre_ref, w_ref, h_ref, p_ref):
        is_ctx = pl.program_id(0) < n_ctx_tiles
        n, _ = _rms(jnp.where(is_ctx, ctx_ref[...], x_ref[...]))
        shift = jnp.where(is_ctx, modc_ref[0:1, :], modx_ref[0:1, :])
        scale = jnp.where(is_ctx, modc_ref[1:2, :], modx_ref[1:2, :])
        h = (n * pre_ref[...] * (1.0 + scale) + shift).astype(BF16)
        h_ref[...] = h
        for s, w in _col_chunks(width):
            p_ref[:, s:s + w] = _dot_nt(h, w_ref[_w_in_row(s):_w_in_row(s) + w, :])

    row = lambda i: (i, 0)
    fixed = lambda i: (0, 0)
    return pl.pallas_call(
        body, name=name, grid=(rows // TM,),
        out_shape=[jax.ShapeDtypeStruct((rows, d), BF16), jax.ShapeDtypeStruct((rows, width), F32)],
        in_specs=[pl.BlockSpec((TM, d), lambda i: (jnp.minimum(i, n_ctx_tiles - 1), 0)),
                  pl.BlockSpec((TM, d), lambda i: (jnp.maximum(i - n_ctx_tiles, 0), 0)),
                  pl.BlockSpec((8, d), fixed), pl.BlockSpec((8, d), fixed), pl.BlockSpec((1, d), fixed), VMEM_SPEC],
        out_specs=[pl.BlockSpec((TM, d), row), pl.BlockSpec((TM, width), row)],
        compiler_params=_cparams(dimension_semantics=("parallel",)),
    )(ctx0, x0, modc, modx, pre1, w_t)


C_HQ, C_HI, C_HF_FW, C_HF_BW, C_HGATE, C_GQ, C_GK, C_GV, C_GGATE = range(9)
OFF_GATE_HG = 9 * HW
OFF_LR = 13 * HW
P_WIDTH = OFF_LR + 128


def _head_norm_fwd(o, w):
    outs, ns, rs = [], [], []
    for h in range(NH):
        n, r = _rms(o[:, h * HD:(h + 1) * HD])
        ns.append(n)
        rs.append(r)
        outs.append(n * w)
    return jnp.concatenate(outs, axis=1), ns, rs


def _mixer_tail(z, o_hg, o_gla, p_hgate, p_ggate, p_gate_hg, p_gate_gla, hg_on, gla_on, wbh, wbg, wout):
    on_hg, n_hg, r_hg = _head_norm_fwd(o_hg, hg_on)
    on_gla, n_gla, r_gla = _head_norm_fwd(o_gla, gla_on)
    og_hg = (on_hg * _silu(p_hgate)).astype(BF16)
    og_gla = (on_gla * _silu(p_ggate)).astype(BF16)
    b_hg = jnp.dot(og_hg, wbh, preferred_element_type=F32)
    b_gla = jnp.dot(og_gla, wbg, preferred_element_type=F32)
    s_hg = _sigmoid(p_gate_hg)
    s_gla = _sigmoid(p_gate_gla)
    merged = (s_hg * b_hg + s_gla * b_gla).astype(BF16)
    y1 = jnp.dot(merged, wout, preferred_element_type=F32)
    return dict(on_hg=on_hg, n_hg=n_hg, r_hg=r_hg, on_gla=on_gla, n_gla=n_gla, r_gla=r_gla, og_hg=og_hg,
                og_gla=og_gla, b_hg=b_hg, b_gla=b_gla, s_hg=s_hg, s_gla=s_gla, merged=merged, y1=y1)


def _mixer_tail_fwd(x_lat, p, o_list, modx, norms, onorms, w_br_hg, w_br_gla, w_out, n_ctx_tiles, name):
    rows, d = x_lat.shape

    def body(x_ref, ofw_hg, obw_hg, ofw_gla, obw_gla, p_hgate, p_ggate, p_ghg_a, p_ghg_b, p_ggla_a, p_ggla_b,
             modx_ref, norm_ref, on_ref, wbh_ref, wbg_ref, wout_ref, z2_ref, y1_ref, mrg_ref, oghg_ref, oggla_ref):
        p_gate_hg = jnp.concatenate([p_ghg_a[...], p_ghg_b[...]], axis=1)
        p_gate_gla = jnp.concatenate([p_ggla_a[...], p_ggla_b[...]], axis=1)
        t = _mixer_tail(x_ref[...], ofw_hg[...] + obw_hg[...], ofw_gla[...] + obw_gla[...], p_hgate[...],
                        p_ggate[...], p_gate_hg, p_gate_gla, on_ref[0:1, 0:HD], on_ref[1:2, 0:HD],
                        wbh_ref[...], wbg_ref[...], wout_ref[...])
        y1_ref[...] = t["y1"]
        mrg_ref[...] = t["merged"]
        oghg_ref[...] = t["og_hg"]
        oggla_ref[...] = t["og_gla"]
        n1, _ = _rms(t["y1"])
        z2_ref[...] = x_ref[...] + n1 * norm_ref[1:2, :] * modx_ref[2:3, :]

    lat = lambda i: (i, 0)
    full = lambda i: (i + n_ctx_tiles, 0)
    fixed = lambda i: (0, 0)

    def pcol(blk):
        return pl.BlockSpec((TM, HW), lambda i: (i + n_ctx_tiles, blk))

    in_specs = ([pl.BlockSpec((TM, d), lat)] + [pl.BlockSpec((TM, HW), full)] * 4
                + [pcol(C_HGATE), pcol(C_GGATE), pcol(9), pcol(10), pcol(11), pcol(12)]
                + [pl.BlockSpec((8, d), fixed)] * 3 + [VMEM_SPEC] * 3)
    bf = lambda w: jax.ShapeDtypeStruct((rows, w), BF16)
    f32 = jax.ShapeDtypeStruct((rows, d), F32)
    return pl.pallas_call(
        body, name=name, grid=(rows // TM,), out_shape=[f32, f32, bf(d), bf(HW), bf(HW)], in_specs=in_specs,
        out_specs=[pl.BlockSpec((TM, d), lat)] * 3 + [pl.BlockSpec((TM, HW), lat)] * 2,
        compiler_params=_cparams(dimension_semantics=("parallel",)),
    )(x_lat, *o_list, p, p, p, p, p, p, modx, norms, onorms, w_br_hg, w_br_gla, w_out)


def _ffn_fwd_bwd(z2, modx, norms, w_gate, w_up, w_down, target, name):
    rows, d = z2.shape
    dff = w_gate.shape[0]
    inv_d = 1.0 / d

    def body(z2_ref, modx_ref, norm_ref, wg_ref, wu_ref, wd_ref, t_ref,
             loss_ref, dz2_ref, h2_ref, a_ref, du_ref, dv_ref, dy2_ref, stat_ref):
        i = pl.program_id(0)
        pre2, post2 = norm_ref[2:3, :], norm_ref[3:4, :]
        shift2, scale2, gate2 = modx_ref[3:4, :], modx_ref[4:5, :], modx_ref[5:6, :]
        z2 = z2_ref[...]
        n2, r2 = _rms(z2)
        nw2 = n2 * pre2
        h2 = (nw2 * (1.0 + scale2) + shift2).astype(BF16)
        h2_ref[...] = h2
        u = _dot_nt(h2, wg_ref[...])
        v = _dot_nt(h2, wu_ref[...])
        su = _silu(u)
        a = (su * v).astype(BF16)
        a_ref[...] = a
        y2 = jnp.dot(a, wd_ref[...], preferred_element_type=F32)
        n3, r3 = _rms(y2)
        err = z2 + n3 * post2 * gate2 - t_ref[...]
        part = 0.5 * inv_d * jnp.sum(err * err)
        dz3 = err * inv_d
        dgate2 = _colsum(dz3 * n3 * post2)
        tt = dz3 * gate2
        dpost2 = _colsum(tt * n3)
        dy2 = _rms_bwd(tt * post2, n3, r3).astype(BF16)
        dy2_ref[...] = dy2
        da = _dot_nt(dy2, wd_ref[...])
        du = (da * v * _dsilu(u)).astype(BF16)
        dv = (da * su).astype(BF16)
        du_ref[...] = du
        dv_ref[...] = dv
        dh2 = (jnp.dot(du, wg_ref[...], preferred_element_type=F32)
               + jnp.dot(dv, wu_ref[...], preferred_element_type=F32))
        dshift2 = _colsum(dh2)
        dscale2 = _colsum(dh2 * nw2)
        dnw2 = dh2 * (1.0 + scale2)
        dpre2 = _colsum(dnw2 * n2)
        dz2_ref[...] = dz3 + _rms_bwd(dnw2 * pre2, n2, r2)

        @pl.when(i == 0)
        def _():
            stat_ref[...] = jnp.zeros_like(stat_ref)
            loss_ref[...] = jnp.zeros_like(loss_ref)

        for r, val in enumerate((dshift2, dscale2, dgate2, dpre2, dpost2)):
            stat_ref[r:r + 1, :] += val
        loss_ref[...] += part
        stat_ref[5:6, 0:128] += part

    lat = lambda i: (i, 0)
    fixed = lambda i: (0, 0)
    bf = lambda w: jax.ShapeDtypeStruct((rows, w), BF16)
    return pl.pallas_call(
        body, name=name, grid=(rows // TM,),
        out_shape=[jax.ShapeDtypeStruct((8, 128), F32), jax.ShapeDtypeStruct((rows, d), F32), bf(d), bf(dff), bf(dff),
                   bf(dff), bf(d), jax.ShapeDtypeStruct((8, d), F32)],
        in_specs=[pl.BlockSpec((TM, d), lat), pl.BlockSpec((8, d), fixed), pl.BlockSpec((8, d), fixed)]
        + [VMEM_SPEC] * 3 + [pl.BlockSpec((TM, d), lat)],
        out_specs=[pl.BlockSpec((8, 128), fixed), pl.BlockSpec((TM, d), lat), pl.BlockSpec((TM, d), lat),
                   pl.BlockSpec((TM, dff), lat), pl.BlockSpec((TM, dff), lat), pl.BlockSpec((TM, dff), lat),
                   pl.BlockSpec((TM, d), lat), pl.BlockSpec((8, d), fixed)],
        compiler_params=_cparams(dimension_semantics=("arbitrary",)),
    )(z2, modx, norms, w_gate, w_up, w_down, target)


def _mixer_tail_bwd(x_lat, p, o_list, dz2, y1, modx, norms, onorms, w_br_hg, w_br_gla, w_out, n_ctx_tiles, n_tiles,
                    name):
    rows, d = x_lat.shape
    total = n_tiles * TM

    def body(x_ref, ofw_hg, obw_hg, ofw_gla, obw_gla, p_hgate, p_ggate, p_ghg_a, p_ghg_b, p_ggla_a, p_ggla_b,
             dz2_ref, y1_ref, modx_ref, norm_ref, on_ref, wbh_ref, wbg_ref, wout_ref,
             dohg_ref, dogla_ref, dhgate_ref, dggate_ref, dghg_ref, dggla_ref, dy1_ref, dbhg_ref, dbgla_ref,
             stat_ref):
        i = pl.program_id(0)

        @pl.when(i == 0)
        def _():
            stat_ref[...] = jnp.zeros_like(stat_ref)

        @pl.when(i < n_ctx_tiles)
        def _():
            for ref in (dohg_ref, dogla_ref, dhgate_ref, dggate_ref, dghg_ref, dggla_ref):
                ref[...] = jnp.zeros_like(ref)

        @pl.when(i >= n_ctx_tiles)
        def _():
            post1, gate1 = norm_ref[1:2, :], modx_ref[2:3, :]
            hg_on, gla_on = on_ref[0:1, 0:HD], on_ref[1:2, 0:HD]
            p_gate_hg = jnp.concatenate([p_ghg_a[...], p_ghg_b[...]], axis=1)
            p_gate_gla = jnp.concatenate([p_ggla_a[...], p_ggla_b[...]], axis=1)
            ph, pg = p_hgate[...], p_ggate[...]
            t = _mixer_tail(x_ref[...], ofw_hg[...] + obw_hg[...], ofw_gla[...] + obw_gla[...], ph, pg,
                            p_gate_hg, p_gate_gla, hg_on, gla_on, wbh_ref[...], wbg_ref[...], wout_ref[...])
            dz2 = dz2_ref[...]
            n1, r1 = _rms(y1_ref[...])
            dgate1 = _colsum(dz2 * n1 * post1)
            tt = dz2 * gate1
            dpost1 = _colsum(tt * n1)
            dy1 = _rms_bwd(tt * post1, n1, r1).astype(BF16)
            dy1_ref[...] = dy1
            dmerged = _dot_nt(dy1, wout_ref[...])
            dghg_ref[...] = (dmerged * t["b_hg"] * t["s_hg"] * (1.0 - t["s_hg"])).astype(BF16)
            dggla_ref[...] = (dmerged * t["b_gla"] * t["s_gla"] * (1.0 - t["s_gla"])).astype(BF16)
            db_hg = (dmerged * t["s_hg"]).astype(BF16)
            db_gla = (dmerged * t["s_gla"]).astype(BF16)
            dbhg_ref[...] = db_hg
            dbgla_ref[...] = db_gla
            don_acc = []
            for (db, wb, pgate, on, ns, rs, gain, gate_ref, do_ref) in (
                    (db_hg, wbh_ref, ph, t["on_hg"], t["n_hg"], t["r_hg"], hg_on, dhgate_ref, dohg_ref),
                    (db_gla, wbg_ref, pg, t["on_gla"], t["n_gla"], t["r_gla"], gla_on, dggate_ref, dogla_ref)):
                dog = _dot_nt(db, wb[...])
                gate_ref[...] = (dog * on * _dsilu(pgate)).astype(BF16)
                don = dog * _silu(pgate)
                acc = jnp.zeros((1, HD), F32)
                for h in range(NH):
                    sl = slice(h * HD, (h + 1) * HD)
                    acc = acc + _colsum(don[:, sl] * ns[h])
                    do_ref[:, sl] = _rms_bwd(don[:, sl] * gain, ns[h], rs[h]).astype(BF16)
                don_acc.append(acc)
            stat_ref[0:1, :] += dgate1
            stat_ref[1:2, :] += dpost1
            stat_ref[2:3, 0:HD] += don_acc[0]
            stat_ref[2:3, HD:2 * HD] += don_acc[1]

    lat = lambda i: (jnp.maximum(i - n_ctx_tiles, 0), 0)
    full = lambda i: (i, 0)
    fixed = lambda i: (0, 0)

    def pcol(blk):
        return pl.BlockSpec((TM, HW), lambda i: (i, blk))

    in_specs = ([pl.BlockSpec((TM, d), lat)] + [pl.BlockSpec((TM, HW), full)] * 4
                + [pcol(C_HGATE), pcol(C_GGATE), pcol(9), pcol(10), pcol(11), pcol(12)]
                + [pl.BlockSpec((TM, d), lat), pl.BlockSpec((TM, d), lat)]
                + [pl.BlockSpec((8, d), fixed)] * 3 + [VMEM_SPEC] * 3)
    f = lambda w: jax.ShapeDtypeStruct((total, w), BF16)
    out_shape = [f(HW), f(HW), f(HW), f(HW), f(d), f(d), jax.ShapeDtypeStruct((rows, d), BF16),
                 jax.ShapeDtypeStruct((rows, d), BF16), jax.ShapeDtypeStruct((rows, d), BF16),
                 jax.ShapeDtypeStruct((8, d), F32)]
    out_specs = ([pl.BlockSpec((TM, HW), full)] * 4 + [pl.BlockSpec((TM, d), full)] * 2
                 + [pl.BlockSpec((TM, d), lat)] * 3 + [pl.BlockSpec((8, d), fixed)])
    return pl.pallas_call(
        body, name=name, grid=(n_tiles,), out_shape=out_shape, in_specs=in_specs, out_specs=out_specs,
        compiler_params=_cparams(dimension_semantics=("arbitrary",)),
    )(x_lat, *o_list, p, p, p, p, p, p, dz2, y1, modx, norms, onorms, w_br_hg, w_br_gla, w_out)


def _in_projection_bwd(ctx0, x0, dz2, modc, modx, pre1, w_t, pieces, n_ctx_tiles, name):
    d = x0.shape[1]
    rows = ctx0.shape[0] + x0.shape[0]
    lat_rows = dz2.shape[0]
    width = P_WIDTH
    n_pieces = len(pieces)

    def body(*refs):
        ctx_ref, x_ref, dz2_ref, modc_ref, modx_ref, pre_ref, w_ref = refs[:7]
        (dhq_f, dhq_b, dhi_f, dhi_b, dhf_f, dhf_b, dhgate, dgq_f, dgq_b, dgk_f, dgk_b, dgv_f, dgv_b, dggate,
         dghg, dggla, dlr_f, dlr_b) = refs[7:7 + n_pieces]
        dp_ref, gx_ref, stat_ref = refs[7 + n_pieces:]
        i = pl.program_id(0)
        is_ctx = i < n_ctx_tiles
        z = jnp.where(is_ctx, ctx_ref[...], x_ref[...])
        sections = [
            (0, dhq_f[...] + dhq_b[...]), (HW, dhi_f[...] + dhi_b[...]), (2 * HW, dhf_f[...]), (3 * HW, dhf_b[...]),
            (4 * HW, dhgate[...]), (5 * HW, dgq_f[...] + dgq_b[...]), (6 * HW, dgk_f[...] + dgk_b[...]),
            (7 * HW, dgv_f[...] + dgv_b[...]), (8 * HW, dggate[...]),
            (9 * HW, dghg[:, 0:HW]), (10 * HW, dghg[:, HW:2 * HW]),
            (11 * HW, dggla[:, 0:HW]), (12 * HW, dggla[:, HW:2 * HW]), (OFF_LR, dlr_f[...] + dlr_b[...])]
        dh = jnp.zeros((TM, d), F32)
        for off, val in sections:
            w = val.shape[1]
            vb = val.astype(BF16)
            dp_ref[:, off:off + w] = vb
            dh = dh + jnp.dot(vb, w_ref[_w_in_row(off):_w_in_row(off) + w, :], preferred_element_type=F32)
        n, r = _rms(z)
        pre = pre_ref[...]
        scale = jnp.where(is_ctx, modc_ref[1:2, :], modx_ref[1:2, :])
        nw = n * pre
        dshift = _colsum(dh)
        dscale = _colsum(dh * nw)
        dnw = dh * (1.0 + scale)
        dpre = _colsum(dnw * n)
        gx_ref[...] = dz2_ref[...] + _rms_bwd(dnw * pre, n, r)
        zero = jnp.zeros((1, d), F32)

        @pl.when(i == 0)
        def _():
            stat_ref[...] = jnp.zeros_like(stat_ref)

        stat_ref[0:1, :] += jnp.where(is_ctx, zero, dshift)
        stat_ref[1:2, :] += jnp.where(is_ctx, zero, dscale)
        stat_ref[2:3, :] += jnp.where(is_ctx, dshift, zero)
        stat_ref[3:4, :] += jnp.where(is_ctx, dscale, zero)
        stat_ref[4:5, :] += dpre

    full = lambda i: (i, 0)
    lat = lambda i: (jnp.maximum(i - n_ctx_tiles, 0), 0)
    fixed = lambda i: (0, 0)
    piece_specs = [pl.BlockSpec((TM, a.shape[1]), full) for a in pieces]
    in_specs = [pl.BlockSpec((TM, d), lambda i: (jnp.minimum(i, n_ctx_tiles - 1), 0)), pl.BlockSpec((TM, d), lat),
                pl.BlockSpec((TM, d), lat), pl.BlockSpec((8, d), fixed),
                pl.BlockSpec((8, d), fixed), pl.BlockSpec((1, d), fixed), VMEM_SPEC] + piece_specs
    return pl.pallas_call(
        body, name=name, grid=(rows // TM,),
        out_shape=[jax.ShapeDtypeStruct((rows, width), BF16), jax.ShapeDtypeStruct((lat_rows, d), F32),
                   jax.ShapeDtypeStruct((8, d), F32)],
        in_specs=in_specs,
        out_specs=[pl.BlockSpec((TM, width), full), pl.BlockSpec((TM, d), lat), pl.BlockSpec((8, d), fixed)],
        compiler_params=_cparams(dimension_semantics=("arbitrary",)),
    )(ctx0, x0, dz2, modc, modx, pre1, w_t, *pieces)


def _transposed_lhs_matmul(x_ref, dy_ref, o_ref, xt_ref):
    @pl.when(pl.program_id(1) == 0)
    def _():
        xt_ref[...] = x_ref[...].T

    o_ref[...] = jnp.dot(xt_ref[...], dy_ref[...], preferred_element_type=F32)


def _w_in_grad(dp, h1, n_cols, name):
    rows, d = h1.shape
    n_main = OFF_LR // HW
    lr0 = _w_in_row(OFF_LR)

    def body(x_ref, xlr_ref, h_ref, o_hbm, xt_ref, acc_ref, sem):
        i = pl.program_id(0)

        def main_copy(step):
            row = jnp.where(step < 9, step * HW, step * HW + 2 * RANK)
            return pltpu.make_async_copy(acc_ref, o_hbm.at[pl.ds(pl.multiple_of(row, 8), HW), :], sem)

        lr_copy = pltpu.make_async_copy(acc_ref.at[0:2 * RANK, :], o_hbm.at[lr0:lr0 + 2 * RANK, :], sem)

        @pl.when(i < n_main)
        def _():
            xt_ref[...] = x_ref[...].T

        @pl.when(i > 0)
        def _():
            main_copy(i - 1).wait()

        @pl.when(i < n_main)
        def _():
            acc_ref[...] = jnp.dot(xt_ref[...], h_ref[...], preferred_element_type=F32)
            main_copy(i).start()

        @pl.when(i == n_main)
        def _():
            xt_ref[0:128, :] = xlr_ref[...].T
            acc_ref[0:128, :] = jnp.dot(xt_ref[0:128, :], h_ref[...], preferred_element_type=F32)
            lr_copy.start()
            lr_copy.wait()

    return pl.pallas_call(
        body, name=name, grid=(n_main + 1,),
        out_shape=jax.ShapeDtypeStruct((n_cols, d), F32),
        in_specs=[pl.BlockSpec((rows, HW), lambda i: (0, jnp.minimum(i, n_main - 1))),
                  pl.BlockSpec((rows, 128), lambda i: (0, OFF_LR // 128)),
                  pl.BlockSpec((rows, d), lambda i: (0, 0))],
        out_specs=ANY_SPEC,
        scratch_shapes=[pltpu.VMEM((HW, rows), BF16), pltpu.VMEM((HW, d), F32), pltpu.SemaphoreType.DMA],
        compiler_params=_cparams(dimension_semantics=("arbitrary",)),
    )(dp, dp, h1)


def _weight_grad(xs, dy, name, tk=None, tn=512, k_first=0, k_tiles=None):
    rows = dy.shape[0]
    n = dy.shape[1]
    tn_ = min(tn, n)
    tk_ = xs.shape[1] if tk is None else tk
    k_tiles = xs.shape[1] // tk_ if k_tiles is None else k_tiles
    k = k_tiles * tk_

    return pl.pallas_call(
        functools.partial(_transposed_lhs_matmul), name=name, grid=(k_tiles, n // tn_),
        out_shape=jax.ShapeDtypeStruct((k, n), F32),
        in_specs=[pl.BlockSpec((rows, tk_), lambda i, j: (0, i + k_first)),
                  pl.BlockSpec((rows, tn_), lambda i, j: (0, j))],
        out_specs=pl.BlockSpec((tk_, tn_), lambda i, j: (i, j)),
        scratch_shapes=[pltpu.VMEM((tk_, rows), BF16)],
        compiler_params=_cparams(dimension_semantics=("parallel", "arbitrary")),
    )(xs, dy)


def _running_sums(xs, fws):
    c = xs[0].shape[0]
    row = lax.broadcasted_iota(jnp.int32, (c, 1), 0)
    s = 1
    while s < c:
        xs = [x + (jnp.where(row >= s, pltpu.roll(x, s, axis=0), 0.0) if fw else
                   jnp.where(row < c - s, pltpu.roll(x, c - s, axis=0), 0.0)) for x, fw in zip(xs, fws)]
        s *= 2
    return xs


def _chunks_terms(qs, ks, gs, fws):
    c = CHUNK
    n = len(qs)
    r = lax.broadcasted_iota(jnp.int32, (c, c), 0)
    s = lax.broadcasted_iota(jnp.int32, (c, c), 1)
    row = lax.broadcasted_iota(jnp.int32, (c, 1), 0)
    per_dir = {}
    for fw in set(fws):
        pos = row if fw else (c - 1 - row)
        per_dir[fw] = dict(
            causal=(s <= r) if fw else (s >= r), causal_t=(s >= r) if fw else (s <= r), pos=pos,
            in_blk=[(pos >= SUB * j) & (pos < SUB * (j + 1)) for j in range(NSUB)],
            start_row=[None] + [SUB * j - 1 if fw else c - SUB * j for j in range(1, NSUB)],
            rend=c - 1 if fw else 0)
    dirs = [per_dir[fw] for fw in fws]
    cums = _running_sums(gs, fws)
    starts = [[None] + [cum[d["start_row"][j]:d["start_row"][j] + 1, :] for j in range(1, NSUB)]
              for cum, d in zip(cums, dirs)]
    es = [[jnp.exp(cum) for cum in cums]]
    for j in range(1, NSUB):
        es.append([jnp.exp(jnp.where(d["pos"] >= SUB * j, cum - st[j], -1e30)) for cum, st, d in zip(cums, starts, dirs)])
    owns = [functools.reduce(lambda rest, j: jnp.where(d["in_blk"][j], st[j], rest), range(1, NSUB), 0.0)
            for st, d in zip(starts, dirs)]
    kscales = [jnp.exp(own - cum) for own, cum in zip(owns, cums)]
    cends = [cum[d["rend"]:d["rend"] + 1, :] for cum, d in zip(cums, dirs)]
    tails = [jnp.exp(cend - cum) for cend, cum in zip(cends, cums)]
    qcats = [jnp.concatenate([q * es[j][i] for j in range(NSUB)], axis=1).astype(BF16) for i, q in enumerate(qs)]
    kts = [k * ksc for k, ksc in zip(ks, kscales)]
    kms = [jnp.concatenate([jnp.where(d["in_blk"][j], kt, 0.0) for j in range(NSUB)], axis=1).astype(BF16)
           for kt, d in zip(kts, dirs)]
    e_by_lane = [[es[j][i] for j in range(NSUB)] for i in range(n)]
    return dict(dirs=dirs, e=e_by_lane, kscale=kscales, cend=cends, tail=tails, qcat=qcats, km=kms, kt=kts)


def _chunks_fwd(qs, ks, vs, gs, st0s, fws):
    t = _chunks_terms(qs, ks, gs, fws)
    scores = [_dot_nt(qc, km) for qc, km in zip(t["qcat"], t["km"])]
    a = [jnp.where(d["causal"], sc, 0.0) for sc, d in zip(scores, t["dirs"])]
    inter = [_dot_nt(qc[:, 0:HD], st0) for qc, st0 in zip(t["qcat"], st0s)]
    intra = [_dot(a_, v) for a_, v in zip(a, vs)]
    os_ = [x + y for x, y in zip(intra, inter)]
    upd = [_dot_tn(v, k * tl) for v, k, tl in zip(vs, ks, t["tail"])]
    st1s = [st0 * jnp.exp(ce) + u for st0, ce, u in zip(st0s, t["cend"], upd)]
    return os_, st1s


def _chunks_bwd(qs, ks, vs, gs, st0s, dos, dst1s, fws):
    n = len(qs)
    t = _chunks_terms(qs, ks, gs, fws)
    qcat, km, e, dirs = t["qcat"], t["km"], t["e"], t["dirs"]
    a_t = [jnp.where(d["causal_t"], _dot_nt(km_, qc), 0.0) for km_, qc, d in zip(km, qcat, dirs)]
    ktail = [k * tl for k, tl in zip(ks, t["tail"])]
    dv_a = [_dot(at, do) for at, do in zip(a_t, dos)]
    dv_b = [_dot_nt(kt, ds) for kt, ds in zip(ktail, dst1s)]
    dv = [x + y for x, y in zip(dv_a, dv_b)]
    da = [jnp.where(d["causal"], _dot_nt(do, v), 0.0) for do, v, d in zip(dos, vs, dirs)]
    da_t = [jnp.where(d["causal_t"], _dot_nt(v, do), 0.0) for do, v, d in zip(dos, vs, dirs)]
    dqcat = [_dot(da_, km_) for da_, km_ in zip(da, km)]
    dq_inter = [e[i][0] * _dot(dos[i], st0s[i]) for i in range(n)]
    dkm = [_dot(dat, qc) for dat, qc in zip(da_t, qcat)]
    dk_inter = [_dot(v, ds) * tl for v, ds, tl in zip(vs, dst1s, t["tail"])]
    dq = [dq_inter[i] + sum(e[i][j] * dqcat[i][:, j * HD:(j + 1) * HD] for j in range(NSUB)) for i in range(n)]
    dkt = [sum(jnp.where(dirs[i]["in_blk"][j], dkm[i][:, j * HD:(j + 1) * HD], 0.0) for j in range(NSUB))
           for i in range(n)]
    dk = [dkt[i] * t["kscale"][i] + dk_inter[i] for i in range(n)]
    dcum = [qs[i] * dq_inter[i] - ks[i] * dk_inter[i] - t["kt"][i].astype(BF16).astype(F32) * dkt[i]
            + sum(qcat[i][:, j * HD:(j + 1) * HD].astype(F32) * dqcat[i][:, j * HD:(j + 1) * HD] for j in range(NSUB))
            for i in range(n)]
    ecend = [jnp.exp(ce) for ce in t["cend"]]
    end = [ecend[i] * _colsum(st0s[i] * dst1s[i]) + _colsum(ks[i] * dk_inter[i]) for i in range(n)]
    sums = _running_sums(dcum, [not fw for fw in fws])
    dg = [sm + en for sm, en in zip(sums, end)]
    upd = [_dot_tn(dos[i], qs[i] * e[i][0]) for i in range(n)]
    dst0 = [dst1s[i] * ecend[i] + upd[i] for i in range(n)]
    return dq, dk, dv, dg, dst0


def _chunk_index(step, n_ctx_chunks, n_chunks, fw):
    if fw:
        return step
    return jnp.where(step < n_ctx_chunks, n_ctx_chunks - 1 - step, n_chunks - 1 + n_ctx_chunks - step)


def _hg_inputs(hq, hf, lbv, d_idx, sl):
    lb = _sigmoid(lbv[d_idx:d_idx + 1, sl] - lbv[2 + d_idx:3 + d_idx, sl])
    sg = _sigmoid(hf)
    f = lb + (1.0 - lb) * sg
    return _silu(hq), 1.0 - f, jnp.log(f), f, sg, lb


def _scan_fwd_both(p, side, n_ctx_chunks, branch, name):
    rows = p.shape[0]
    n_chunks = rows // CHUNK
    hg = branch == "hg"
    n_in = 4 if hg else 6

    def body(*refs):
        ins, outs, state = refs[:2 * n_in], refs[2 * n_in:2 * n_in + 4], refs[-1]

        @pl.when(pl.program_id(0) == 0)
        def _():
            state[...] = jnp.zeros_like(state)

        lanes, where = [], []
        for di, fw in enumerate((True, False)):
            r = ins[di * n_in:(di + 1) * n_in]
            o_ref, st_ref = outs[2 * di], outs[2 * di + 1]
            if hg:
                a_ref, b_ref, c_ref, lb_ref = r
            else:
                a_ref, b_ref, c_ref, lr_ref, wgk_ref, bgk_ref = r
                logits = _dot(lr_ref[...], wgk_ref[...]) + bgk_ref[...]
                g_all = _log_sigmoid(logits) * (1.0 / GATE_NORM)
            for h in range(NH):
                sl = slice(h * HD, (h + 1) * HD)
                if hg:
                    q, k, g, _, _, _ = _hg_inputs(a_ref[:, sl], c_ref[:, sl], lb_ref[...], di, sl)
                    v = b_ref[:, sl]
                else:
                    q, k, v, g = a_ref[:, sl] * (HD ** -0.5), b_ref[:, sl], c_ref[:, sl], g_all[:, sl]
                lanes.append((q, k, v, g, state[di, h], fw))
                where.append((di, h, sl, o_ref, st_ref))
        qs, ks, vs, gs, st0s, fws = (list(col) for col in zip(*lanes))
        os_, st1s = _chunks_fwd(qs, ks, vs, gs, st0s, fws)
        for (di, h, sl, o_ref, st_ref), st0, o, st1 in zip(where, st0s, os_, st1s):
            st_ref[0, h] = st0
            o_ref[:, sl] = o
            state[di, h] = st1

    fixed = lambda j: (0, 0)
    in_specs, args, out_specs = [], [], []
    for di, fw in enumerate((True, False)):
        chunk = functools.partial(_chunk_index, n_ctx_chunks=n_ctx_chunks, n_chunks=n_chunks, fw=fw)

        def cmap(blk, width=HW, chunk=chunk):
            return pl.BlockSpec((CHUNK, width), lambda j: (chunk(j), blk))

        if hg:
            in_specs += [cmap(C_HQ), cmap(C_HI), cmap(C_HF_FW + di), pl.BlockSpec((4, HW), fixed)]
            args += [p, p, p, side]
        else:
            in_specs += [cmap(C_GQ), cmap(C_GK), cmap(C_GV), cmap(OFF_LR // 128, 128),
                         pl.BlockSpec((128, HW), fixed), pl.BlockSpec((1, HW), fixed)]
            args += [p, p, p, p, side[di][0], side[di][1]]
        out_specs += [cmap(0), pl.BlockSpec((1, NH, HD, HD), lambda j, chunk=chunk: (chunk(j), 0, 0, 0))]
    return pl.pallas_call(
        body, name=name, grid=(n_chunks,),
        out_shape=[jax.ShapeDtypeStruct((rows, HW), F32), jax.ShapeDtypeStruct((n_chunks, NH, HD, HD), F32)] * 2,
        in_specs=in_specs, out_specs=out_specs,
        scratch_shapes=[pltpu.VMEM((2, NH, HD, HD), F32)],
        compiler_params=_cparams(dimension_semantics=("arbitrary",)),
    )(*args)


def _scan_bwd_both(p, side, states, d_o, n_ctx_chunks, branch, name):
    rows = p.shape[0]
    n_chunks = rows // CHUNK
    hg = branch == "hg"
    n_in = 6 if hg else 8
    n_out = 4 if hg else 6

    def body(*refs):
        ins, outs, dstate = refs[:2 * n_in], refs[2 * n_in:2 * n_in + 2 * n_out], refs[-1]
        first = pl.program_id(0) == 0

        @pl.when(first)
        def _():
            dstate[...] = jnp.zeros_like(dstate)

        lanes, where, extra, ctx = [], [], [], []
        for di, fw in enumerate((True, False)):
            r, w = ins[di * n_in:(di + 1) * n_in], outs[di * n_out:(di + 1) * n_out]
            if hg:
                a_ref, b_ref, c_ref, lb_ref, st_ref, do_ref = r
                da_ref, db_ref, dc_ref, dlb_ref = w
                acc_refs = (dlb_ref,)
            else:
                a_ref, b_ref, c_ref, lr_ref, wgk_ref, bgk_ref, st_ref, do_ref = r
                da_ref, db_ref, dc_ref, dlr_ref, dwgk_ref, dbias_ref = w
                acc_refs = (dwgk_ref, dbias_ref)
                lr = lr_ref[...]
                logits = _dot(lr, wgk_ref[...]) + bgk_ref[...]
                g_all = _log_sigmoid(logits) * (1.0 / GATE_NORM)

            @pl.when(first)
            def _(acc_refs=acc_refs):
                for ref in acc_refs:
                    ref[...] = jnp.zeros_like(ref)

            for h in range(NH):
                sl = slice(h * HD, (h + 1) * HD)
                if hg:
                    hq, hf = a_ref[:, sl], c_ref[:, sl]
                    q, k, g, f, sg, lb = _hg_inputs(hq, hf, lb_ref[...], di, sl)
                    v = b_ref[:, sl]
                    extra.append((hq, f, sg, lb))
                else:
                    q, k, v, g = a_ref[:, sl] * (HD ** -0.5), b_ref[:, sl], c_ref[:, sl], g_all[:, sl]
                    extra.append(None)
                lanes.append((q, k, v, g, st_ref[0, h], do_ref[:, sl], dstate[di, h], fw))
                where.append((di, h, sl))
            ctx.append((w, None if hg else (lr, logits, wgk_ref)))

        dqs, dks, dvs, dgs, dst0s = [], [], [], [], []
        for lo in range(0, len(lanes), BWD_LANES):
            cols = [list(col) for col in zip(*lanes[lo:lo + BWD_LANES])]
            for acc, part in zip((dqs, dks, dvs, dgs, dst0s), _chunks_bwd(*cols)):
                acc.extend(part)
        dg_parts = {0: [], 1: []}
        for (di, h, sl), ex, dq, dk, dv, dg, dst0 in zip(where, extra, dqs, dks, dvs, dgs, dst0s):
            dstate[di, h] = dst0
            w = ctx[di][0]
            if hg:
                hq, f, sg, lb = ex
                da_ref, db_ref, dc_ref, dlb_ref = w
                da_ref[:, sl] = (dq * _dsilu(hq)).astype(BF16)
                db_ref[:, sl] = dv.astype(BF16)
                df = dg / f - dk
                dc_ref[:, sl] = (df * (1.0 - lb) * sg * (1.0 - sg)).astype(BF16)
                dlb_ref[0:1, sl] += _colsum(df * (1.0 - sg))
            else:
                da_ref, db_ref, dc_ref = w[:3]
                da_ref[:, sl] = (dq * (HD ** -0.5)).astype(BF16)
                db_ref[:, sl] = dk.astype(BF16)
                dc_ref[:, sl] = dv.astype(BF16)
                dg_parts[di].append(dg)
        if not hg:
            for di in range(2):
                dlr_ref, dwgk_ref, dbias_ref = ctx[di][0][3:]
                lr, logits, wgk_ref = ctx[di][1]
                dlogits = jnp.concatenate(dg_parts[di], axis=1) * (1.0 / GATE_NORM) * (1.0 - _sigmoid(logits))
                dlr_ref[...] = _dot_nt(dlogits, wgk_ref[...]).astype(BF16)
                dwgk_ref[...] += _dot_tn(lr, dlogits)
                dbias_ref[0:1, :] += _colsum(dlogits)

    fixed = lambda j: (0, 0)
    big = jax.ShapeDtypeStruct((rows, HW), BF16)
    in_specs, args, out_shape, out_specs = [], [], [], []
    for di, fw in enumerate((True, False)):
        def chunk_of(j, fw=fw):
            return _chunk_index(n_chunks - 1 - j, n_ctx_chunks, n_chunks, fw)

        def cmap(blk, width=HW, chunk_of=chunk_of):
            return pl.BlockSpec((CHUNK, width), lambda j: (chunk_of(j), blk))

        st_spec = pl.BlockSpec((1, NH, HD, HD), lambda j, chunk_of=chunk_of: (chunk_of(j), 0, 0, 0))
        if hg:
            in_specs += [cmap(C_HQ), cmap(C_HI), cmap(C_HF_FW + di), pl.BlockSpec((4, HW), fixed), st_spec, cmap(0)]
            args += [p, p, p, side, states[di], d_o]
            out_shape += [big, big, big, jax.ShapeDtypeStruct((8, HW), F32)]
            out_specs += [cmap(0), cmap(0), cmap(0), pl.BlockSpec((8, HW), fixed)]
        else:
            in_specs += [cmap(C_GQ), cmap(C_GK), cmap(C_GV), cmap(OFF_LR // 128, 128),
                         pl.BlockSpec((128, HW), fixed), pl.BlockSpec((1, HW), fixed), st_spec, cmap(0)]
            args += [p, p, p, p, side[di][0], side[di][1], states[di], d_o]
            out_shape += [big, big, big, jax.ShapeDtypeStruct((rows, 128), BF16),
                          jax.ShapeDtypeStruct((128, HW), F32), jax.ShapeDtypeStruct((8, HW), F32)]
            out_specs += [cmap(0), cmap(0), cmap(0), cmap(0, 128), pl.BlockSpec((128, HW), fixed),
                          pl.BlockSpec((8, HW), fixed)]
    return pl.pallas_call(
        body, name=name, grid=(n_chunks,), out_shape=out_shape, in_specs=in_specs, out_specs=out_specs,
        scratch_shapes=[pltpu.VMEM((2, NH, HD, HD), F32)],
        compiler_params=_cparams(dimension_semantics=("arbitrary",)),
    )(*args)


SMALL_ROWS = 56
ROWS_MOD_X = (0, 1, 8, 16, 17, 18)
ROWS_MOD_C = (2, 3)
ROW_PRE1, ROW_POST1, ROW_ONORM, ROW_PRE2, ROW_POST2, ROW_LB, ROW_BGK, ROW_WGK = 4, 9, 10, 19, 20, 24, 32, 40
ROW_LOSS = 21


def _reduce_small(gathered, lb_full, name):
    _, _, d = gathered.shape

    def body(g_ref, lb_ref, sum_ref, dmod_ref, dbmod_ref, dlb_ref):
        total = g_ref[0]
        for b in range(1, N_DEV):
            total = total + g_ref[b]
        sum_ref[...] = total
        dmod_ref[...] = jnp.zeros_like(dmod_ref)
        for m in range(N_MOD):
            col = slice(m * d, (m + 1) * d)
            acc = jnp.zeros((1, d), F32)
            for b in range(N_DEV):
                row = g_ref[b, ROWS_MOD_X[m]:ROWS_MOD_X[m] + 1, :]
                dmod_ref[b:b + 1, col] = row
                acc = acc + row
            if m < 2:
                ctx_row = total[ROWS_MOD_C[m]:ROWS_MOD_C[m] + 1, :]
                dmod_ref[8:9, col] = ctx_row
                acc = acc + ctx_row
            dbmod_ref[:, col] = acc
        lbv = lb_ref[...]
        for dd in range(2):
            lb = _sigmoid(lbv[dd:dd + 1, :] - lbv[2 + dd:3 + dd, :])
            gl = total[ROW_LB:ROW_LB + 1, dd * HW:(dd + 1) * HW] * lb * (1.0 - lb)
            dlb_ref[dd:dd + 1, :] = gl
            dlb_ref[2 + dd:3 + dd, :] = -gl

    return pl.pallas_call(
        body, name=name,
        out_shape=[jax.ShapeDtypeStruct((SMALL_ROWS, d), F32), jax.ShapeDtypeStruct((16, N_MOD * d), F32),
                   jax.ShapeDtypeStruct((1, N_MOD * d), F32), jax.ShapeDtypeStruct((4, HW), F32)],
        in_specs=[VMEM_SPEC] * 2, out_specs=[VMEM_SPEC] * 4, compiler_params=_cparams(),
    )(gathered, lb_full)


def _c_ctx_grad(gathered, c_ctx_row, name):
    def body(g_ref, c_ref, o_ref):
        acc = g_ref[0, 0:1, :]
        for chip in range(1, N_CHIP):
            acc = acc + g_ref[2 * chip, 0:1, :]
        o_ref[...] = acc * _dsilu(c_ref[...])

    return pl.pallas_call(
        body, name=name, out_shape=jax.ShapeDtypeStruct(c_ctx_row.shape, F32),
        in_specs=[VMEM_SPEC] * 2, out_specs=VMEM_SPEC, compiler_params=_cparams(),
    )(gathered, c_ctx_row)


def _blocked(full, n_blocks):
    k, n = full.shape
    return full.reshape(k, n_blocks, n // n_blocks).transpose(1, 0, 2)


def _unblocked(blocks):
    nb, k, n = blocks.shape
    return blocks.transpose(1, 0, 2).reshape(k, nb * n)


def _sample_front(x0, ctx0, modc, modx, norm_pre1, lb_full, gla_side, w_in_r):
    ctx_len = ctx0.shape[0]
    n_ctx_tiles = ctx_len // TM
    n_ctx_chunks = ctx_len // CHUNK
    h1, p = _in_projection(ctx0, x0, modc, modx, norm_pre1, w_in_r, n_ctx_tiles, "in_projection")
    o_hg_fw, st_hg_fw, o_hg_bw, st_hg_bw = _scan_fwd_both(p, lb_full, n_ctx_chunks, "hg", "scan_hg")
    o_gla_fw, st_gla_fw, o_gla_bw, st_gla_bw = _scan_fwd_both(p, gla_side, n_ctx_chunks, "gla", "scan_gla")
    return dict(h1=h1, p=p, o_list=[o_hg_fw, o_hg_bw, o_gla_fw, o_gla_bw],
                states=[st_hg_fw, st_hg_bw, st_gla_fw, st_gla_bw])


def _sample_back(reduce, front, x0, ctx0, target0, modc, modx, norm_pre1, norms, onorms, lb_full, gla_side, w_in_r,
                 wbh, wbg, wout, ffn_weights):
    seq, d = x0.shape
    ctx_len = ctx0.shape[0]
    n_ctx_tiles = ctx_len // TM
    n_tiles = (ctx_len + seq) // TM
    n_ctx_chunks = ctx_len // CHUNK
    h1, p, o_list = front["h1"], front["p"], front["o_list"]
    st_hg_fw, st_hg_bw, st_gla_fw, st_gla_bw = front["states"]
    z2, y1, merged, og_hg, og_gla = _mixer_tail_fwd(x0, p, o_list, modx, norms, onorms, wbh, wbg, wout, n_ctx_tiles,
                                                    "mixer_tail")
    wg, wu, wd = ffn_weights([z2])
    loss_part, dz2, h2, a_act, du, dv, dy2, stat_ffn = _ffn_fwd_bwd(z2, modx, norms, wg, wu, wd, target0, "ffn")
    dff = wg.shape[0]
    tok = reduce("ffn", [_weight_grad(du, h2, "grad_w_ff_gate", tk=dff // 2, tn=d),
                         _weight_grad(dv, h2, "grad_w_ff_up", tk=dff // 2, tn=d),
                         _weight_grad(a_act, dy2, "grad_w_ff_down", tk=dff // 2)])

    (d_ohg, d_ogla, d_hgate, d_ggate, d_ghg, d_ggla, dy1, db_hg, db_gla, stat_mix) = _mixer_tail_bwd(
        x0, p, o_list, dz2, y1, modx + tok, norms, onorms, wbh, wbg, wout, n_ctx_tiles, n_tiles, "mixer_tail_bwd")
    tok = reduce("mix", [_weight_grad(og_hg, db_hg, "grad_w_br_hg"), _weight_grad(og_gla, db_gla, "grad_w_br_gla"),
                         _weight_grad(merged, dy1, "grad_w_out")])
    tok = tok + reduce("push_ffn", [dy1])
    gla_b = [(wgk, bias + tok) for wgk, bias in gla_side]
    (dgq_f, dgk_f, dgv_f, dlr_f, dwgk_f, dbgk_f, dgq_b, dgk_b, dgv_b, dlr_b, dwgk_b, dbgk_b) = _scan_bwd_both(
        p, gla_b, (st_gla_fw, st_gla_bw), d_ogla, n_ctx_chunks, "gla", "scan_gla_bwd")
    lb_b = lb_full + reduce("push_mix", [dbgk_f])
    (dhq_f, dhi_f, dhf_f, dlb_f, dhq_b, dhi_b, dhf_b, dlb_b) = _scan_bwd_both(
        p, lb_b, (st_hg_fw, st_hg_bw), d_ohg, n_ctx_chunks, "hg", "scan_hg_bwd")
    pieces = [dhq_f, dhq_b, dhi_f, dhi_b, dhf_f, dhf_b, d_hgate, dgq_f, dgq_b, dgk_f, dgk_b, dgv_f, dgv_b, d_ggate,
              d_ghg, d_ggla, dlr_f, dlr_b]
    dp, grad_x, stat_in = _in_projection_bwd(ctx0, x0, dz2, modc, modx, norm_pre1, w_in_r, pieces, n_ctx_tiles,
                                             "in_projection_bwd")

    tok = reduce("in", [_w_in_grad(dp, h1, w_in_r.shape[0], "grad_w_in")])
    reduce("small", dict(stat_in=stat_in + tok, stat_mix=stat_mix, stat_ffn=stat_ffn, dlb=(dlb_f, dlb_b),
                         dwgk=(dwgk_f, dwgk_b), dbgk=(dbgk_f, dbgk_b)))
    reduce("push_in", [])
    return dict(loss_part=loss_part, grad_x=grad_x)


def kernel(x, c, ctx, c_ctx, w_mod, b_mod, norm_pre1, norm_post1, norm_pre2, norm_post2, w_in, hg_lb, hg_onorm, gla_w_gk, gla_b_gk, gla_onorm, w_br_hg, w_br_gla, w_out, w_ff_gate, w_ff_up, w_ff_down, loss_target, m_c_ctx, m_w_mod, m_b_mod, m_norm_pre1, m_norm_post1, m_norm_pre2, m_norm_post2, m_w_in, m_hg_lb, m_hg_onorm, m_gla_w_gk, m_gla_b_gk, m_gla_onorm, m_w_br_hg, m_w_br_gla, m_w_out, m_w_ff_gate, m_w_ff_up, m_w_ff_down, v_c_ctx, v_w_mod, v_b_mod, v_norm_pre1, v_norm_post1, v_norm_pre2, v_norm_post2, v_w_in, v_hg_lb, v_hg_onorm, v_gla_w_gk, v_gla_b_gk, v_gla_onorm, v_w_br_hg, v_w_br_gla, v_w_out, v_w_ff_gate, v_w_ff_up, v_w_ff_down):
    seq, d = x.shape[1], x.shape[2]
    ctx_len = ctx.shape[1]
    assert seq % TM == 0 and ctx_len % TM == 0 and d == 2 * HW
    ax, ay, ac = lax.axis_index("x"), lax.axis_index("y"), lax.axis_index("c")
    chip = 2 * ax + ay
    dev = 2 * chip + ac
    c_arr = jnp.reshape(ac, (1,)).astype(jnp.int32)
    chip_arr = jnp.reshape(chip, (1,)).astype(jnp.int32)
    transposed = ("w_in", "w_ff_gate", "w_ff_up")
    view = lambda a, nm: a[0].T if nm in transposed else a[0]

    sems_in, lands_in, token_in0 = _blocks_start([_cast_into_blocks(chip_arr, view(w_in, "w_in"), "cast_w_in")],
                                                 "gather_w_in_start")

    nc = d // 128
    pad8 = lambda a: jnp.pad(a, ((0, -a.shape[0] % 8), (0, 0)))
    small1 = jnp.concatenate([c.reshape(nc, 128) + token_in0[0, 0], pad8(hg_lb.reshape(4, 128)),
                              gla_w_gk.reshape(2 * RANK, 128), pad8(gla_b_gk.reshape(2, 128))], axis=0)
    blocks = [_cast_into_blocks(chip_arr, view(w_, nm), "cast_" + nm) for w_, nm in (
        (w_br_hg, "w_br_hg"), (w_br_gla, "w_br_gla"), (w_out, "w_out"), (w_ff_gate, "w_ff_gate"),
        (w_ff_up, "w_ff_up"), (w_ff_down, "w_ff_down"))]
    got1 = _allgather8(small1, "gather_small_params", after=blocks)
    c_all = got1[:, :nc, :].reshape(N_DEV, d)
    per_chip = got1[0::2]
    lb_full = per_chip[:, nc:nc + 4, :].transpose(1, 0, 2).reshape(4, HW)
    wgk_full = per_chip[:, nc + 8:nc + 8 + 2 * RANK, :].transpose(1, 0, 2).reshape(2, RANK, HW)
    bgk_full = per_chip[:, nc + 8 + 2 * RANK:nc + 10 + 2 * RANK, :].transpose(1, 0, 2).reshape(2, HW)
    wgk_pad = [jnp.zeros((128, HW), F32).at[dd * RANK:(dd + 1) * RANK].set(wgk_full[dd]) for dd in range(2)]
    bgk = [bgk_full[dd:dd + 1] for dd in range(2)]

    n_mod_cols = w_mod.shape[2]
    cond = jnp.concatenate([c_all, pad8(c_ctx.reshape(1, d))], axis=0)
    b_cols = lax.dynamic_slice(b_mod, (0, chip * n_mod_cols), (1, n_mod_cols))
    lands_in = _blocks_wait(sems_in, lands_in, [got1], "gather_w_in_wait")
    fwd_sems, lands_in, fwd_token = _forward_start(lands_in, "gather_w_in_forward_start")
    mod_part = _mod_forward(cond + fwd_token[0, 0], w_mod[0], b_cols, "mod_forward")
    mod_got = _allgather8(mod_part, "gather_mod")
    mod_all = mod_got[0::2].transpose(1, 0, 2).reshape(16, N_CHIP * n_mod_cols)
    modx = pad8(lax.dynamic_slice(mod_all, (dev, 0), (1, N_MOD * d)).reshape(N_MOD, d))
    modc = pad8(mod_all[8].reshape(N_MOD, d))

    gathered_in = _forward_wait(fwd_sems, lands_in, [mod_got], "gather_w_in_forward_wait")
    sems, lands, token = _blocks_start(blocks, "gather_rest_start", after=[gathered_in[0]])
    w_in_r = gathered_in[0].reshape(-1, d)

    norms = jnp.concatenate([norm_pre1, norm_post1, norm_pre2, norm_post2, jnp.zeros((4, d), F32)], axis=0)
    onorms = jnp.zeros((8, d), F32).at[0, :HD].set(hg_onorm[0]).at[1, :HD].set(gla_onorm[0])
    gla_side = [(wgk_pad[dd], bgk[dd]) for dd in range(2)]
    modx = modx + token[0, 0]
    front = _sample_front(x[0], ctx[0], modc, modx, norm_pre1, lb_full, gla_side, w_in_r)
    lands = _blocks_wait(sems, lands, front["o_list"], "gather_rest_wait")
    gathered = _blocks_finish(lands[:3], "gather_mix_finish")
    wbh, wbg = _unblocked(gathered[0]), _unblocked(gathered[1])
    wout = gathered[2].reshape(d, d)
    ffn_sems, ffn_lands, ffn_token = _forward_start(lands[3:], "gather_ffn_forward_start")
    onorms = onorms + ffn_token[0, 0]

    def ffn_weights(after):
        got = _forward_wait(ffn_sems, ffn_lands, after, "gather_ffn_forward_wait")
        return tuple(g.reshape(-1, d) for g in got)

    dff = w_ff_down.shape[1] * N_CHIP
    groups = {"ffn": ["w_ff_gate", "w_ff_up", "w_ff_down"], "mix": ["w_br_hg", "w_br_gla", "w_out"], "in": ["w_in"]}
    row_sharded = {"w_out": d // N_CHIP, "w_ff_down": dff // N_CHIP, "w_ff_gate": dff // N_CHIP,
                   "w_ff_up": dff // N_CHIP, "w_in": w_in.shape[2]}
    in_flight, to_sibling, small = {}, {}, {}

    def reduce_small(stats):
        small2 = jnp.concatenate([
            stats["stat_in"], stats["stat_mix"], stats["stat_ffn"],
            jnp.concatenate(stats["dlb"], axis=1), jnp.concatenate(stats["dbgk"], axis=1),
            jnp.concatenate([stats["dwgk"][0][0:RANK], stats["dwgk"][1][RANK:2 * RANK]], axis=1)], axis=0)
        assert small2.shape[0] == SMALL_ROWS
        got2 = _allgather8(small2, "gather_small_grads")
        total, dmod_all, g_b_mod, g_lb_full = _reduce_small(got2, lb_full, "reduce_small")
        dmod_cols = lax.dynamic_slice(dmod_all, (0, chip * n_mod_cols), (16, n_mod_cols))
        g_w_mod, cctx_part = _mod_backward(cond, w_mod[0], dmod_cols, "mod_backward")
        got3 = _allgather8(cctx_part, "gather_c_ctx_grad")
        g_c_ctx = _c_ctx_grad(got3, c_ctx.reshape(1, d), "c_ctx_grad")
        small.update(total=total, g_b_mod=g_b_mod, g_lb_full=g_lb_full, g_w_mod=g_w_mod, g_c_ctx=g_c_ctx)

    def reduce(group, grads):
        if group == "small":
            return reduce_small(grads)
        if group.startswith("push_"):
            return push(group[5:], grads)
        nms = groups[group]
        full = [g.reshape(N_CHIP, row_sharded[nm], d) if nm in row_sharded else _blocked(g, N_CHIP)
                for g, nm in zip(grads, nms)]
        sems_, full, lands_, token_ = _send_half_start(full, "grads_to_sibling_start_" + group)
        to_sibling[group] = (sems_, full, lands_)
        return token_[0, 0]

    def push(group, after):
        nms = groups[group]
        sems_, full, lands_ = to_sibling[group]
        if group == "in":
            after = list(after) + [small["g_c_ctx"], small["total"]]
        full, from_sibling = _send_half_wait(sems_, full, lands_, after, "grads_to_sibling_wait_" + group)
        pairs = [_pair_sum(c_arr, f, r_, "pair_sum_" + nm) for f, r_, nm in zip(full, from_sibling, nms)]
        after = [small["g_c_ctx"], small["total"]] if group == "in" else []
        sems_, pairs, lands_, token_ = _scatter_start(pairs, "grads_to_owner_start_" + group, after)
        in_flight[group] = (sems_, pairs, lands_, token_)
        return token_[0, 0]

    r = _sample_back(reduce, front, x[0], ctx[0], loss_target[0], modc, modx, norm_pre1, norms, onorms, lb_full,
                     gla_side, w_in_r, wbh, wbg, wout, ffn_weights)
    grad_x = r["grad_x"]

    weights = dict(w_in=(w_in, m_w_in, v_w_in), w_br_hg=(w_br_hg, m_w_br_hg, v_w_br_hg),
                   w_br_gla=(w_br_gla, m_w_br_gla, v_w_br_gla), w_out=(w_out, m_w_out, v_w_out),
                   w_ff_gate=(w_ff_gate, m_w_ff_gate, v_w_ff_gate), w_ff_up=(w_ff_up, m_w_ff_up, v_w_ff_up),
                   w_ff_down=(w_ff_down, m_w_ff_down, v_w_ff_down))
    names = ["w_in", "w_br_hg", "w_br_gla", "w_out", "w_ff_gate", "w_ff_up", "w_ff_down"]
    big, swapping = {}, {}

    def sum_and_swap(group, after):
        sems_, pairs, lands_, _ = in_flight[group]
        pairs, lands_ = _scatter_wait(sems_, pairs, lands_, after, "grads_to_owner_wait_" + group)
        own_half = [_sum_owner(chip_arr, pr, g, "chip_sum_" + nm) for pr, g, nm in zip(pairs, lands_, groups[group])]
        swapping[group] = _swap_start(own_half, "halves_to_sibling_start_" + group)
        return own_half[-1]

    def update(group, after):
        sems_, own_half, lands_ = swapping[group]
        own_half, other_half = _swap_wait(sems_, own_half, lands_, after, "halves_to_sibling_wait_" + group)
        done = []
        for nm, own, oth in zip(groups[group], own_half, other_half):
            w_, m_, v_ = (view(a, nm) for a in weights[nm])
            res = _adamw_halves(c_arr, own, oth, w_, m_, v_, "adamw_" + nm)
            big[nm] = [r_.T[None] if nm in transposed else r_[None] for r_ in res]
            done.append(res[1])
        return done

    token_in = in_flight["in"][3]
    summed_ffn = sum_and_swap("ffn", [token_in])
    summed_mix = sum_and_swap("mix", [summed_ffn])

    total, g_b_mod, g_lb_full, g_w_mod, g_c_ctx = (small[k] for k in ("total", "g_b_mod", "g_lb_full", "g_w_mod",
                                                                      "g_c_ctx"))
    g_pre1, g_post1, g_pre2, g_post2 = (total[r_:r_ + 1] for r_ in (ROW_PRE1, ROW_POST1, ROW_PRE2, ROW_POST2))
    g_hg_on, g_gla_on = total[ROW_ONORM:ROW_ONORM + 1, 0:HD], total[ROW_ONORM:ROW_ONORM + 1, HD:2 * HD]
    n_lb = hg_lb.shape[2]
    g_hg_lb = lax.dynamic_slice(g_lb_full, (0, chip * n_lb), (4, n_lb))
    g_bgk = lax.dynamic_slice(total[ROW_BGK:ROW_BGK + 1].reshape(2, HW), (0, chip * n_lb), (2, n_lb))
    g_wgk_full = total[ROW_WGK:ROW_WGK + RANK].reshape(RANK, 2, HW).transpose(1, 0, 2).reshape(2 * RANK, HW)
    g_wgk = lax.dynamic_slice(g_wgk_full, (0, chip * n_lb), (2 * RANK, n_lb))

    small_items = [
        (g_c_ctx, c_ctx.reshape(1, d), m_c_ctx.reshape(1, d), v_c_ctx.reshape(1, d)),
        (g_b_mod, b_mod, m_b_mod, v_b_mod),
        (g_pre1, norm_pre1, m_norm_pre1, v_norm_pre1),
        (g_post1, norm_post1, m_norm_post1, v_norm_post1),
        (g_pre2, norm_pre2, m_norm_pre2, v_norm_pre2),
        (g_post2, norm_post2, m_norm_post2, v_norm_post2),
        (g_hg_lb, hg_lb.reshape(4, n_lb), m_hg_lb.reshape(4, n_lb), v_hg_lb.reshape(4, n_lb)),
        (g_hg_on, hg_onorm, m_hg_onorm, v_hg_onorm),
        (g_wgk, gla_w_gk.reshape(2 * RANK, n_lb), m_gla_w_gk.reshape(2 * RANK, n_lb), v_gla_w_gk.reshape(2 * RANK, n_lb)),
        (g_bgk, gla_b_gk.reshape(2, n_lb), m_gla_b_gk.reshape(2, n_lb), v_gla_b_gk.reshape(2, n_lb)),
        (g_gla_on, gla_onorm, m_gla_onorm, v_gla_onorm),
    ]
    small_res = _adamw_whole(small_items, "adamw_small")
    mod_res = _adamw_tiled(g_w_mod, w_mod[0], m_w_mod[0], v_w_mod[0], "adamw_w_mod")
    done_ffn = update("ffn", [summed_mix, mod_res[0], small_res[0][0]])
    done_mix = update("mix", done_ffn)
    update("in", [sum_and_swap("in", done_mix)])

    loss = total[ROW_LOSS, 0]

    shapes = dict(c_ctx=c_ctx.shape, b_mod=b_mod.shape, norm_pre1=norm_pre1.shape, norm_post1=norm_post1.shape,
                  norm_pre2=norm_pre2.shape, norm_post2=norm_post2.shape, hg_lb=hg_lb.shape, hg_onorm=hg_onorm.shape,
                  gla_w_gk=gla_w_gk.shape, gla_b_gk=gla_b_gk.shape, gla_onorm=gla_onorm.shape)
    small_names = ["c_ctx", "b_mod", "norm_pre1", "norm_post1", "norm_pre2", "norm_post2", "hg_lb", "hg_onorm",
                   "gla_w_gk", "gla_b_gk", "gla_onorm"]
    grads, deltas, new_m, new_v = {}, {}, {}, {}
    for nm, item, res in zip(small_names, small_items, small_res):
        grads[nm] = item[0].reshape(shapes[nm])
        deltas[nm], new_m[nm], new_v[nm] = (r_.reshape(shapes[nm]) for r_ in res)
    grads["w_mod"] = g_w_mod[None]
    deltas["w_mod"], new_m["w_mod"], new_v["w_mod"] = (r_[None] for r_ in mod_res)
    for nm in names:
        grads[nm], deltas[nm], new_m[nm], new_v[nm] = big[nm]
    order = ["c_ctx", "w_mod", "b_mod", "norm_pre1", "norm_post1", "norm_pre2", "norm_post2", "w_in", "hg_lb",
             "hg_onorm", "gla_w_gk", "gla_b_gk", "gla_onorm", "w_br_hg", "w_br_gla", "w_out", "w_ff_gate", "w_ff_up",
             "w_ff_down"]
    return (loss, grad_x[None], *[grads[n] for n in order], *[deltas[n] for n in order],
            *[new_m[n] for n in order], *[new_v[n] for n in order])
```

```python
import functools

import jax
import jax.numpy as jnp
from jax import lax
from jax.experimental import pallas as pl
from jax.experimental.pallas import tpu as pltpu

F32 = jnp.float32
BF16 = jnp.bfloat16
MESH = pl.DeviceIdType.MESH

EPS = 1e-6
CHUNK = 64
SUB = 16
NSUB = CHUNK // SUB
NH = 4
HD = 128
HW = NH * HD
RANK = 16
GATE_NORM = 16.0
N_MOD = 6
TM = 256
BWD_LANES = 8
N_DEV = 8
N_CHIP = 4
VMEM_LIMIT = 56 * 1024 * 1024

ADAM_LR = 0.001
ADAM_B1 = 0.9
ADAM_B2 = 0.999
ADAM_EPS = 1e-08
ADAM_WD = 0.01
ADAM_STEP = 10

VMEM_SPEC = pl.BlockSpec(memory_space=pltpu.VMEM)
ANY_SPEC = pl.BlockSpec(memory_space=pl.ANY)
HBM_SPEC = pl.BlockSpec(memory_space=pltpu.HBM)
SEM_SPEC = pl.BlockSpec(memory_space=pltpu.SEMAPHORE)
EFFECT = pltpu.SideEffectType.DATAFLOW_SIDE_EFFECTING


def _cparams(**kw):
    return pltpu.CompilerParams(vmem_limit_bytes=VMEM_LIMIT, **kw)


def _dot(a, b):
    return jnp.dot(a.astype(BF16), b.astype(BF16), preferred_element_type=F32)


def _dot_nt(a, b):
    return lax.dot_general(a.astype(BF16), b.astype(BF16), (((1,), (1,)), ((), ())), preferred_element_type=F32)


def _dot_tn(a, b):
    return lax.dot_general(a.astype(BF16), b.astype(BF16), (((0,), (0,)), ((), ())), preferred_element_type=F32)


def _sigmoid(x):
    return 1.0 / (1.0 + jnp.exp(-x))


def _silu(x):
    return x * _sigmoid(x)


def _dsilu(x):
    s = _sigmoid(x)
    return s * (1.0 + x * (1.0 - s))


def _log_sigmoid(x):
    return jnp.minimum(x, 0.0) - jnp.log(1.0 + jnp.exp(-jnp.abs(x)))


def _colsum(a):
    return jnp.sum(a, axis=0, keepdims=True)


def _rms(a):
    r = lax.rsqrt(jnp.mean(a * a, axis=-1, keepdims=True) + EPS)
    return a * r, r


def _rms_bwd(dn, n, r):
    return r * (dn - n * jnp.mean(dn * n, axis=-1, keepdims=True))


def _place():
    x, y, c = lax.axis_index("x"), lax.axis_index("y"), lax.axis_index("c")
    chips = [(1 - x, y), (x, 1 - y), (1 - x, 1 - y)]
    return x, y, c, chips


def _allgather8(v, name, after=()):
    rows, cols = v.shape
    n_after = len(after)

    def body(x_ref, *rest):
        out_ref, send_sems, recv_sems, local_sem = rest[n_after:]
        x, y, c, chips = _place()
        me, sibling = (x, y, c), (x, y, 1 - c)

        def blk(px, py, pc):
            return out_ref.at[4 * px + 2 * py + pc]

        def copy(k, block, to, src=None):
            return pltpu.make_async_remote_copy(
                src_ref=blk(*block) if src is None else src, dst_ref=blk(*block),
                send_sem=send_sems.at[k], recv_sem=recv_sems.at[k], device_id=to, device_id_type=MESH)

        mine = pltpu.make_async_copy(x_ref, blk(*me), local_sem)
        mine.start()
        first = [copy(0, me, sibling, src=x_ref)]
        first += [copy(1 + j, me, (*chip, c), src=x_ref) for j, chip in enumerate(chips)]
        for cp in first:
            cp.start()
        passed = [copy(4 + j, (*chip, c), sibling) for j, chip in enumerate(chips)]
        for j, chip in enumerate(chips):
            copy(1 + j, (*chip, c), me).wait_recv()
            passed[j].start()
        copy(0, sibling, me).wait_recv()
        for j, chip in enumerate(chips):
            copy(4 + j, (*chip, 1 - c), me).wait_recv()
        for cp in first + passed:
            cp.wait_send()
        mine.wait()

    return pl.pallas_call(
        body, name=name,
        out_shape=jax.ShapeDtypeStruct((N_DEV, rows, cols), v.dtype),
        in_specs=[VMEM_SPEC] + [ANY_SPEC] * n_after, out_specs=VMEM_SPEC,
        scratch_shapes=[pltpu.SemaphoreType.DMA((7,)), pltpu.SemaphoreType.DMA((7,)), pltpu.SemaphoreType.DMA],
    )(v, *after)


def _cast_into_blocks(chip_arr, w, name):
    rows, cols = w.shape
    tr = _row_tile(rows, 16, 256)

    def body(chip_ref, w_ref, o_ref):
        o_ref[0] = w_ref[...].astype(BF16)

    return pl.pallas_call(
        body, name=name,
        grid_spec=pltpu.PrefetchScalarGridSpec(
            num_scalar_prefetch=1, grid=(rows // tr,),
            in_specs=[pl.BlockSpec((tr, cols), lambda i, chip_ref: (i, 0))],
            out_specs=pl.BlockSpec((1, tr, cols), lambda i, chip_ref: (chip_ref[0], i, 0))),
        out_shape=jax.ShapeDtypeStruct((N_CHIP, rows, cols), BF16),
        compiler_params=_cparams(dimension_semantics=("parallel",)),
    )(chip_arr, w)


def _halved_by_rows(shape):
    return (shape[1] // 2) % 16 == 0


def _half_of(ref, pc, block=None):
    lead = slice(None) if block is None else block
    if _halved_by_rows(ref.shape):
        h = ref.shape[1] // 2
        return ref.at[lead, pl.ds(pl.multiple_of(pc * h, 16), h), :]
    h = ref.shape[2] // 2
    return ref.at[lead, :, pl.ds(pl.multiple_of(pc * h, 128), h)]


def _half_shape(shape):
    return (shape[0], shape[1] // 2, shape[2]) if _halved_by_rows(shape) else (shape[0], shape[1], shape[2] // 2)


def _half_rows(ref, chip_id, pc):
    return _half_of(ref, pc, chip_id)


def _hbm(a):
    return pltpu.with_memory_space_constraint(a, pltpu.HBM)


def _blocks_start(lands, name, after=()):
    n = len(lands)
    n_sem = 3 * n
    first = n + len(after)

    def body(*refs):
        lnd = refs[:n]
        send_sems, recv_sems = refs[first:first + n_sem], refs[first + n_sem:first + 2 * n_sem]
        token = refs[-1]
        x, y, c, chips = _place()
        me_chip = 2 * x + y
        for k in range(n):
            for j, chip in enumerate(chips):
                pltpu.make_async_remote_copy(
                    src_ref=_half_rows(lnd[k], me_chip, c), dst_ref=_half_rows(lnd[k], me_chip, c),
                    send_sem=send_sems[3 * k + j], recv_sem=recv_sems[3 * k + j],
                    device_id=(*chip, c), device_id_type=MESH).start()
        token[...] = jnp.zeros_like(token)

    out = pl.pallas_call(
        body, name=name,
        out_shape=(*[pltpu.SemaphoreType.DMA(())] * (2 * n_sem),
                   *[pltpu.HBM(l.shape, l.dtype) for l in lands],
                   jax.ShapeDtypeStruct((8, 128), F32)),
        in_specs=[HBM_SPEC] * n + [ANY_SPEC] * len(after),
        out_specs=(*[SEM_SPEC] * (2 * n_sem), *[HBM_SPEC] * n, VMEM_SPEC),
        input_output_aliases={i: 2 * n_sem + i for i in range(n)},
        compiler_params=pltpu.CompilerParams(has_side_effects=EFFECT),
    )(*[_hbm(l) for l in lands], *after)
    return list(out[:2 * n_sem]), list(out[2 * n_sem:2 * n_sem + n]), out[-1]


def _blocks_wait(sems, lands, after, name):
    n = len(lands)
    n_sem = 3 * n

    def body(*refs):
        lnd = refs[:n]
        s_sems, r_sems = refs[n:n + n_sem], refs[n + n_sem:n + 2 * n_sem]
        x, y, c, chips = _place()
        me_chip = 2 * x + y
        for k in range(n):
            for j, (px, py) in enumerate(chips):
                cp = pltpu.make_async_remote_copy(
                    src_ref=_half_rows(lnd[k], me_chip, c), dst_ref=_half_rows(lnd[k], 2 * px + py, c),
                    send_sem=s_sems[3 * k + j], recv_sem=r_sems[3 * k + j],
                    device_id=(px, py, c), device_id_type=MESH)
                cp.wait_send()
                cp.wait_recv()

    out = pl.pallas_call(
        body, name=name,
        out_shape=tuple(pltpu.HBM(l.shape, l.dtype) for l in lands),
        in_specs=[HBM_SPEC] * n + [SEM_SPEC] * (2 * n_sem) + [ANY_SPEC] * len(after),
        out_specs=[HBM_SPEC] * n,
        input_output_aliases={i: i for i in range(n)},
        compiler_params=pltpu.CompilerParams(has_side_effects=EFFECT),
    )(*lands, *sems, *after)
    return list(out)


def _forward_start(lands, name):
    n = len(lands)
    n_sem = 3 * n

    def body(*refs):
        lnd = refs[:n]
        send_sems, recv_sems = refs[n:n + n_sem], refs[n + n_sem:n + 2 * n_sem]
        x, y, c, chips = _place()
        for k in range(n):
            for j, (px, py) in enumerate(chips):
                pltpu.make_async_remote_copy(
                    src_ref=_half_rows(lnd[k], 2 * px + py, c), dst_ref=_half_rows(lnd[k], 2 * px + py, c),
                    send_sem=send_sems[3 * k + j], recv_sem=recv_sems[3 * k + j],
                    device_id=(x, y, 1 - c), device_id_type=MESH).start()
        refs[-1][...] = jnp.zeros_like(refs[-1])

    out = pl.pallas_call(
        body, name=name,
        out_shape=(*[pltpu.SemaphoreType.DMA(())] * (2 * n_sem), *[pltpu.HBM(l.shape, l.dtype) for l in lands],
                   jax.ShapeDtypeStruct((8, 128), F32)),
        in_specs=[HBM_SPEC] * n,
        out_specs=(*[SEM_SPEC] * (2 * n_sem), *[HBM_SPEC] * n, VMEM_SPEC),
        input_output_aliases={i: 2 * n_sem + i for i in range(n)},
        compiler_params=pltpu.CompilerParams(has_side_effects=EFFECT),
    )(*[_hbm(l) for l in lands])
    return list(out[:2 * n_sem]), list(out[2 * n_sem:2 * n_sem + n]), out[-1]


def _forward_wait(sems, lands, after, name):
    n = len(lands)
    n_sem = 3 * n

    def body(*refs):
        lnd = refs[:n]
        s_sems, r_sems = refs[n:n + n_sem], refs[n + n_sem:n + 2 * n_sem]
        x, y, c, chips = _place()
        for k in range(n):
            for j, (px, py) in enumerate(chips):
                cp = pltpu.make_async_remote_copy(
                    src_ref=_half_rows(lnd[k], 2 * px + py, c), dst_ref=_half_rows(lnd[k], 2 * px + py, 1 - c),
                    send_sem=s_sems[3 * k + j], recv_sem=r_sems[3 * k + j],
                    device_id=(x, y, 1 - c), device_id_type=MESH)
                cp.wait_send()
                cp.wait_recv()

    out = pl.pallas_call(
        body, name=name,
        out_shape=tuple(pltpu.HBM(l.shape, l.dtype) for l in lands),
        in_specs=[HBM_SPEC] * n + [SEM_SPEC] * (2 * n_sem) + [ANY_SPEC] * len(after),
        out_specs=[HBM_SPEC] * n,
        input_output_aliases={i: i for i in range(n)},
        compiler_params=pltpu.CompilerParams(has_side_effects=EFFECT),
    )(*lands, *sems, *after)
    return list(out)


def _blocks_finish(lands, name):
    n = len(lands)

    def body(*refs):
        lnd = refs[n:2 * n]
        send_sems, recv_sems = refs[2 * n:]
        x, y, c, chips = _place()
        sibling = (x, y, 1 - c)

        def copy(k, j, chip_id, pc):
            return pltpu.make_async_remote_copy(
                src_ref=_half_rows(lnd[k], chip_id, pc), dst_ref=_half_rows(lnd[k], chip_id, pc),
                send_sem=send_sems.at[k, j], recv_sem=recv_sems.at[k, j], device_id=sibling, device_id_type=MESH)

        started = []
        for k in range(n):
            for j, (px, py) in enumerate(chips):
                cp = copy(k, j, 2 * px + py, c)
                cp.start()
                started.append(cp)
        for k in range(n):
            for j, (px, py) in enumerate(chips):
                copy(k, j, 2 * px + py, 1 - c).wait_recv()
        for cp in started:
            cp.wait_send()

    out = pl.pallas_call(
        body, name=name,
        out_shape=[jax.ShapeDtypeStruct(l.shape, l.dtype) for l in lands],
        in_specs=[ANY_SPEC] * n, out_specs=[ANY_SPEC] * n,
        input_output_aliases={i: i for i in range(n)},
        scratch_shapes=[pltpu.SemaphoreType.DMA((n, 3)), pltpu.SemaphoreType.DMA((n, 3))],
    )(*lands)
    return list(out)


def _send_half_start(arrs, name):
    n = len(arrs)

    def body(*refs):
        ins, lnd = refs[:n], refs[n:2 * n]
        send_sems, recv_sems = refs[2 * n:3 * n], refs[3 * n:4 * n]
        token = refs[-1]
        x, y, c, _ = _place()
        for k in range(n):
            pltpu.make_async_remote_copy(
                src_ref=_half_of(ins[k], 1 - c), dst_ref=lnd[k], send_sem=send_sems[k], recv_sem=recv_sems[k],
                device_id=(x, y, 1 - c), device_id_type=MESH).start()
        token[...] = jnp.zeros_like(token)

    lands = [_hbm(lax.empty(_half_shape(a.shape), a.dtype)) for a in arrs]
    out = pl.pallas_call(
        body, name=name,
        out_shape=(*[pltpu.SemaphoreType.DMA(())] * (2 * n), *[pltpu.HBM(a.shape, a.dtype) for a in arrs],
                   *[pltpu.HBM(l.shape, l.dtype) for l in lands], jax.ShapeDtypeStruct((8, 128), F32)),
        in_specs=[HBM_SPEC] * (2 * n),
        out_specs=(*[SEM_SPEC] * (2 * n), *[HBM_SPEC] * (2 * n), VMEM_SPEC),
        input_output_aliases={i: 2 * n + i for i in range(2 * n)},
        compiler_params=pltpu.CompilerParams(has_side_effects=EFFECT),
    )(*[_hbm(a) for a in arrs], *lands)
    return list(out[:2 * n]), list(out[2 * n:3 * n]), list(out[3 * n:4 * n]), out[-1]


def _send_half_wait(sems, arrs, lands, after, name):
    n = len(arrs)

    def body(*refs):
        ins, lnd = refs[:n], refs[n:2 * n]
        s_sems, r_sems = refs[2 * n:3 * n], refs[3 * n:4 * n]
        x, y, c, _ = _place()
        for k in range(n):
            cp = pltpu.make_async_remote_copy(
                src_ref=_half_of(ins[k], 1 - c), dst_ref=lnd[k], send_sem=s_sems[k], recv_sem=r_sems[k],
                device_id=(x, y, 1 - c), device_id_type=MESH)
            cp.wait_send()
            cp.wait_recv()

    out = pl.pallas_call(
        body, name=name,
        out_shape=tuple(pltpu.HBM(a.shape, a.dtype) for a in list(arrs) + list(lands)),
        in_specs=[HBM_SPEC] * (2 * n) + [SEM_SPEC] * (2 * n) + [ANY_SPEC] * len(after),
        out_specs=[HBM_SPEC] * (2 * n),
        input_output_aliases={i: i for i in range(2 * n)},
        compiler_params=pltpu.CompilerParams(has_side_effects=EFFECT),
    )(*arrs, *lands, *sems, *after)
    return list(out[:n]), list(out[n:])


def _scatter_start(arrs, name, after=()):
    n = len(arrs)
    n_sem = 3 * n
    first = 2 * n + len(after)

    def body(*refs):
        ins, lnd = refs[:n], refs[n:2 * n]
        send_sems, recv_sems = refs[first:first + n_sem], refs[first + n_sem:first + 2 * n_sem]
        token = refs[-1]
        x, y, c, chips = _place()
        me_chip = 2 * x + y
        for k in range(n):
            for j, (px, py) in enumerate(chips):
                pltpu.make_async_remote_copy(
                    src_ref=ins[k].at[2 * px + py], dst_ref=lnd[k].at[me_chip],
                    send_sem=send_sems[3 * k + j], recv_sem=recv_sems[3 * k + j],
                    device_id=(px, py, c), device_id_type=MESH).start()
        token[...] = jnp.zeros_like(token)

    lands = [_hbm(lax.empty(a.shape, a.dtype)) for a in arrs]
    out = pl.pallas_call(
        body, name=name,
        out_shape=(*[pltpu.SemaphoreType.DMA(())] * (2 * n_sem),
                   *[pltpu.HBM(a.shape, a.dtype) for a in arrs], *[pltpu.HBM(a.shape, a.dtype) for a in arrs],
                   jax.ShapeDtypeStruct((8, 128), F32)),
        in_specs=[HBM_SPEC] * (2 * n) + [ANY_SPEC] * len(after),
        out_specs=(*[SEM_SPEC] * (2 * n_sem), *[HBM_SPEC] * (2 * n), VMEM_SPEC),
        input_output_aliases={i: 2 * n_sem + i for i in range(2 * n)},
        compiler_params=pltpu.CompilerParams(has_side_effects=EFFECT),
    )(*[_hbm(a) for a in arrs], *lands, *after)
    base = 2 * n_sem
    return list(out[:base]), list(out[base:base + n]), list(out[base + n:base + 2 * n]), out[-1]


def _scatter_wait(sems, arrs, lands, after, name):
    n = len(arrs)
    n_sem = 3 * n

    def body(*refs):
        ins, lnd = refs[:n], refs[n:2 * n]
        s_sems, r_sems = refs[2 * n:2 * n + n_sem], refs[2 * n + n_sem:2 * n + 2 * n_sem]
        x, y, c, chips = _place()
        for k in range(n):
            for j, (px, py) in enumerate(chips):
                cp = pltpu.make_async_remote_copy(
                    src_ref=ins[k].at[2 * px + py], dst_ref=lnd[k].at[2 * px + py],
                    send_sem=s_sems[3 * k + j], recv_sem=r_sems[3 * k + j],
                    device_id=(px, py, c), device_id_type=MESH)
                cp.wait_send()
                cp.wait_recv()

    out = pl.pallas_call(
        body, name=name,
        out_shape=tuple(pltpu.HBM(a.shape, a.dtype) for a in list(arrs) + list(lands)),
        in_specs=[HBM_SPEC] * (2 * n) + [SEM_SPEC] * (2 * n_sem) + [ANY_SPEC] * len(after),
        out_specs=[HBM_SPEC] * (2 * n),
        input_output_aliases={i: i for i in range(2 * n)},
        compiler_params=pltpu.CompilerParams(has_side_effects=EFFECT),
    )(*arrs, *lands, *sems, *after)
    return list(out[:n]), list(out[n:])


def _sum_owner(chip_arr, pairs, got, name):
    nb, h, cols = got.shape
    tr = _row_tile(h, 16, 256)

    def body(chip_ref, own_ref, a_ref, b_ref, c_ref, o_ref):
        o_ref[...] = ((own_ref[0].astype(F32) + a_ref[0].astype(F32)) + b_ref[0].astype(F32)) + c_ref[0].astype(F32)

    def slot(off):
        return pl.BlockSpec((1, tr, cols), lambda i, chip_ref: ((chip_ref[0] + off) % N_CHIP, i, 0))

    return pl.pallas_call(
        body, name=name,
        grid_spec=pltpu.PrefetchScalarGridSpec(
            num_scalar_prefetch=1, grid=(h // tr,),
            in_specs=[slot(0), slot(1), slot(2), slot(3)],
            out_specs=pl.BlockSpec((tr, cols), lambda i, chip_ref: (i, 0))),
        out_shape=jax.ShapeDtypeStruct((h, cols), F32),
        compiler_params=_cparams(dimension_semantics=("parallel",)),
    )(chip_arr, pairs, got, got, got)


def _swap_start(arrs, name, after=()):
    n = len(arrs)
    first = 2 * n + len(after)

    def body(*refs):
        ins, lnd = refs[:n], refs[n:2 * n]
        send_sems, recv_sems = refs[first:first + n], refs[first + n:first + 2 * n]
        x, y, c, _ = _place()
        for k in range(n):
            pltpu.make_async_remote_copy(
                src_ref=ins[k], dst_ref=lnd[k], send_sem=send_sems[k], recv_sem=recv_sems[k],
                device_id=(x, y, 1 - c), device_id_type=MESH).start()

    lands = [_hbm(lax.empty(a.shape, a.dtype)) for a in arrs]
    out = pl.pallas_call(
        body, name=name,
        out_shape=(*[pltpu.SemaphoreType.DMA(())] * (2 * n), *[pltpu.HBM(a.shape, a.dtype) for a in arrs],
                   *[pltpu.HBM(a.shape, a.dtype) for a in arrs]),
        in_specs=[HBM_SPEC] * (2 * n) + [ANY_SPEC] * len(after),
        out_specs=(*[SEM_SPEC] * (2 * n), *[HBM_SPEC] * (2 * n)),
        input_output_aliases={i: 2 * n + i for i in range(2 * n)},
        compiler_params=pltpu.CompilerParams(has_side_effects=EFFECT),
    )(*[_hbm(a) for a in arrs], *lands, *after)
    return list(out[:2 * n]), list(out[2 * n:3 * n]), list(out[3 * n:4 * n])


def _swap_wait(sems, arrs, lands, after, name):
    n = len(arrs)

    def body(*refs):
        ins, lnd = refs[:n], refs[n:2 * n]
        s_sems, r_sems = refs[2 * n:3 * n], refs[3 * n:4 * n]
        x, y, c, _ = _place()
        for k in range(n):
            cp = pltpu.make_async_remote_copy(
                src_ref=ins[k], dst_ref=lnd[k], send_sem=s_sems[k], recv_sem=r_sems[k],
                device_id=(x, y, 1 - c), device_id_type=MESH)
            cp.wait_send()
            cp.wait_recv()

    out = pl.pallas_call(
        body, name=name,
        out_shape=tuple(pltpu.HBM(a.shape, a.dtype) for a in list(arrs) + list(lands)),
        in_specs=[HBM_SPEC] * (2 * n) + [SEM_SPEC] * (2 * n) + [ANY_SPEC] * len(after),
        out_specs=[HBM_SPEC] * (2 * n),
        input_output_aliases={i: i for i in range(2 * n)},
        compiler_params=pltpu.CompilerParams(has_side_effects=EFFECT),
    )(*arrs, *lands, *sems, *after)
    return list(out[:n]), list(out[n:])


def _row_tile(h, mult=8, cap=128):
    for t in range(cap - cap % mult, mult - 1, -mult):
        if h % t == 0:
            return t
    if mult > 8:
        return _row_tile(h, 8, cap)
    raise ValueError(h)


def _pair_sum(c_arr, full, recv, name):
    nb, rows, cols = full.shape

    def body(c_ref, f_ref, r_ref, o_ref):
        o_ref[...] = (f_ref[...] + r_ref[...]).astype(BF16)

    if _halved_by_rows(full.shape):
        h = rows // 2
        tr = _row_tile(h, 16, 256)
        steps = h // tr
        own = pl.BlockSpec((1, tr, cols), lambda b, i, c_ref: (b, c_ref[0] * steps + i, 0))
        half = pl.BlockSpec((1, tr, cols), lambda b, i, c_ref: (b, i, 0))
    else:
        steps = 1
        own = pl.BlockSpec((1, rows, cols // 2), lambda b, i, c_ref: (b, 0, c_ref[0]))
        half = pl.BlockSpec((1, rows, cols // 2), lambda b, i, c_ref: (b, 0, 0))
    return pl.pallas_call(
        body, name=name,
        grid_spec=pltpu.PrefetchScalarGridSpec(
            num_scalar_prefetch=1, grid=(nb, steps), in_specs=[own, half], out_specs=half),
        out_shape=jax.ShapeDtypeStruct(_half_shape(full.shape), BF16),
        compiler_params=_cparams(dimension_semantics=("parallel", "parallel")),
    )(c_arr, full, recv)


def _adam_math(g, w, m, v):
    m1 = ADAM_B1 * m + (1.0 - ADAM_B1) * g
    v1 = ADAM_B2 * v + (1.0 - ADAM_B2) * (g * g)
    m_hat = m1 / (1.0 - ADAM_B1 ** ADAM_STEP)
    v_hat = v1 / (1.0 - ADAM_B2 ** ADAM_STEP)
    delta = -ADAM_LR * (m_hat / (jnp.sqrt(v_hat) + ADAM_EPS) + ADAM_WD * w)
    return delta, m1, v1


def _adamw_halves(c_arr, own, other, w, m, v, name):
    rows, cols = w.shape
    by_rows = own.shape[1] == cols

    def body(c_ref, own_ref, oth_ref, w_ref, m_ref, v_ref, g_out, d_out, m_out, v_out):
        if by_rows:
            g = jnp.where(pl.program_id(0) == c_ref[0], own_ref[...], oth_ref[...])
        else:
            own_, oth_ = own_ref[...], oth_ref[...]
            g = jnp.where(c_ref[0] == 0, jnp.concatenate([own_, oth_], axis=1), jnp.concatenate([oth_, own_], axis=1))
        d, m1, v1 = _adam_math(g, w_ref[...], m_ref[...], v_ref[...])
        g_out[...] = g
        d_out[...] = d
        m_out[...] = m1
        v_out[...] = v1

    if by_rows:
        h = rows // 2
        tr = _row_tile(h)
        steps = h // tr
        grid = (2, steps)
        half_spec = pl.BlockSpec((tr, cols), lambda p, i, c_ref: (i, 0))
        full_spec = pl.BlockSpec((tr, cols), lambda p, i, c_ref: (p * steps + i, 0))
    else:
        tr = _row_tile(rows)
        grid = (1, rows // tr)
        half_spec = pl.BlockSpec((tr, cols // 2), lambda p, i, c_ref: (i, 0))
        full_spec = pl.BlockSpec((tr, cols), lambda p, i, c_ref: (i, 0))
    return pl.pallas_call(
        body, name=name,
        grid_spec=pltpu.PrefetchScalarGridSpec(
            num_scalar_prefetch=1, grid=grid,
            in_specs=[half_spec, half_spec, full_spec, full_spec, full_spec],
            out_specs=[full_spec] * 4),
        out_shape=[jax.ShapeDtypeStruct(w.shape, F32)] * 4,
        compiler_params=_cparams(dimension_semantics=("parallel", "parallel")),
    )(c_arr, own, other, w, m, v)


def _adamw_whole(items, name):
    n = len(items)

    def body(*refs):
        ins, outs = refs[:4 * n], refs[4 * n:]
        for k in range(n):
            g, w, m, v = (r[...] for r in ins[4 * k:4 * k + 4])
            d, m1, v1 = _adam_math(g, w, m, v)
            outs[3 * k][...] = d
            outs[3 * k + 1][...] = m1
            outs[3 * k + 2][...] = v1

    flat = [a for it in items for a in it]
    shapes = [jax.ShapeDtypeStruct(it[1].shape, F32) for it in items for _ in range(3)]
    out = pl.pallas_call(
        body, name=name, out_shape=shapes,
        in_specs=[VMEM_SPEC] * (4 * n), out_specs=[VMEM_SPEC] * (3 * n),
        compiler_params=_cparams(),
    )(*flat)
    return [tuple(out[3 * k:3 * k + 3]) for k in range(n)]


def _adamw_tiled(g, w, m, v, name):
    rows, cols = w.shape
    tr = _row_tile(rows)

    def body(g_ref, w_ref, m_ref, v_ref, d_out, m_out, v_out):
        d, m1, v1 = _adam_math(g_ref[...], w_ref[...], m_ref[...], v_ref[...])
        d_out[...] = d
        m_out[...] = m1
        v_out[...] = v1

    spec = pl.BlockSpec((tr, cols), lambda i: (i, 0))
    return pl.pallas_call(
        body, name=name, grid=(rows // tr,),
        out_shape=[jax.ShapeDtypeStruct(w.shape, F32)] * 3,
        in_specs=[spec] * 4, out_specs=[spec] * 3,
        compiler_params=_cparams(dimension_semantics=("parallel",)),
    )(g, w, m, v)


def _mod_forward(cond, w_mod, b_mod_cols, name):
    def body(c_ref, w_ref, b_ref, o_ref):
        o_ref[...] = _dot(_silu(c_ref[...]), w_ref[...]) + b_ref[...]

    return pl.pallas_call(
        body, name=name, out_shape=jax.ShapeDtypeStruct((cond.shape[0], w_mod.shape[1]), F32),
        in_specs=[VMEM_SPEC] * 3, out_specs=VMEM_SPEC, compiler_params=_cparams(),
    )(cond, w_mod, b_mod_cols)


def _mod_backward(cond, w_mod, dmod_cols, name):
    def body(c_ref, w_ref, d_ref, gw_ref, gc_ref):
        s = _silu(c_ref[...])
        d = d_ref[...]
        gw_ref[...] = _dot_tn(s, d)
        gc_ref[...] = _dot_nt(d[8:16, :], w_ref[...])

    return pl.pallas_call(
        body, name=name,
        out_shape=[jax.ShapeDtypeStruct(w_mod.shape, F32), jax.ShapeDtypeStruct((8, w_mod.shape[0]), F32)],
        in_specs=[VMEM_SPEC] * 3, out_specs=[VMEM_SPEC] * 2, compiler_params=_cparams(),
    )(cond, w_mod, dmod_cols)


def _col_chunks(width, step=512):
    return [(s, min(step, width - s)) for s in range(0, width, step)]


def _w_in_row(p_off):
    if p_off < 9 * HW:
        return p_off
    return 9 * HW if p_off == OFF_LR else p_off + 2 * RANK


def _in_projection(ctx0, x0, modc, modx, pre1, w_t, n_ctx_tiles, name):
    d = x0.shape[1]
    rows = ctx0.shape[0] + x0.shape[0]
    width = P_WIDTH

    def body(ctx_ref, x_ref, modc_ref, modx_ref, pre_ref, w_ref, h_ref, p_ref):
        is_ctx = pl.program_id(0) < n_ctx_tiles
        n, _ = _rms(jnp.where(is_ctx, ctx_ref[...], x_ref[...]))
        shift = jnp.where(is_ctx, modc_ref[0:1, :], modx_ref[0:1, :])
        scale = jnp.where(is_ctx, modc_ref[1:2, :], modx_ref[1:2, :])
        h = (n * pre_ref[...] * (1.0 + scale) + shift).astype(BF16)
        h_ref[...] = h
        for s, w in _col_chunks(width):
            p_ref[:, s:s + w] = _dot_nt(h, w_ref[_w_in_row(s):_w_in_row(s) + w, :])

    row = lambda i: (i, 0)
    fixed = lambda i: (0, 0)
    return pl.pallas_call(
        body, name=name, grid=(rows // TM,),
        out_shape=[jax.ShapeDtypeStruct((rows, d), BF16), jax.ShapeDtypeStruct((rows, width), F32)],
        in_specs=[pl.BlockSpec((TM, d), lambda i: (jnp.minimum(i, n_ctx_tiles - 1), 0)),
                  pl.BlockSpec((TM, d), lambda i: (jnp.maximum(i - n_ctx_tiles, 0), 0)),
                  pl.BlockSpec((8, d), fixed), pl.BlockSpec((8, d), fixed), pl.BlockSpec((1, d), fixed), VMEM_SPEC],
        out_specs=[pl.BlockSpec((TM, d), row), pl.BlockSpec((TM, width), row)],
        compiler_params=_cparams(dimension_semantics=("parallel",)),
    )(ctx0, x0, modc, modx, pre1, w_t)


C_HQ, C_HI, C_HF_FW, C_HF_BW, C_HGATE, C_GQ, C_GK, C_GV, C_GGATE = range(9)
OFF_GATE_HG = 9 * HW
OFF_LR = 13 * HW
P_WIDTH = OFF_LR + 128


def _head_norm_fwd(o, w):
    outs, ns, rs = [], [], []
    for h in range(NH):
        n, r = _rms(o[:, h * HD:(h + 1) * HD])
        ns.append(n)
        rs.append(r)
        outs.append(n * w)
    return jnp.concatenate(outs, axis=1), ns, rs


def _mixer_tail(z, o_hg, o_gla, p_hgate, p_ggate, p_gate_hg, p_gate_gla, hg_on, gla_on, wbh, wbg, wout):
    on_hg, n_hg, r_hg = _head_norm_fwd(o_hg, hg_on)
    on_gla, n_gla, r_gla = _head_norm_fwd(o_gla, gla_on)
    og_hg = (on_hg * _silu(p_hgate)).astype(BF16)
    og_gla = (on_gla * _silu(p_ggate)).astype(BF16)
    b_hg = jnp.dot(og_hg, wbh, preferred_element_type=F32)
    b_gla = jnp.dot(og_gla, wbg, preferred_element_type=F32)
    s_hg = _sigmoid(p_gate_hg)
    s_gla = _sigmoid(p_gate_gla)
    merged = (s_hg * b_hg + s_gla * b_gla).astype(BF16)
    y1 = jnp.dot(merged, wout, preferred_element_type=F32)
    return dict(on_hg=on_hg, n_hg=n_hg, r_hg=r_hg, on_gla=on_gla, n_gla=n_gla, r_gla=r_gla, og_hg=og_hg,
                og_gla=og_gla, b_hg=b_hg, b_gla=b_gla, s_hg=s_hg, s_gla=s_gla, merged=merged, y1=y1)


def _mixer_tail_fwd(x_lat, p, o_list, modx, norms, onorms, w_br_hg, w_br_gla, w_out, n_ctx_tiles, name):
    rows, d = x_lat.shape

    def body(x_ref, ofw_hg, obw_hg, ofw_gla, obw_gla, p_hgate, p_ggate, p_ghg_a, p_ghg_b, p_ggla_a, p_ggla_b,
             modx_ref, norm_ref, on_ref, wbh_ref, wbg_ref, wout_ref, z2_ref, y1_ref, mrg_ref, oghg_ref, oggla_ref):
        p_gate_hg = jnp.concatenate([p_ghg_a[...], p_ghg_b[...]], axis=1)
        p_gate_gla = jnp.concatenate([p_ggla_a[...], p_ggla_b[...]], axis=1)
        t = _mixer_tail(x_ref[...], ofw_hg[...] + obw_hg[...], ofw_gla[...] + obw_gla[...], p_hgate[...],
                        p_ggate[...], p_gate_hg, p_gate_gla, on_ref[0:1, 0:HD], on_ref[1:2, 0:HD],
                        wbh_ref[...], wbg_ref[...], wout_ref[...])
        y1_ref[...] = t["y1"]
        mrg_ref[...] = t["merged"]
        oghg_ref[...] = t["og_hg"]
        oggla_ref[...] = t["og_gla"]
        n1, _ = _rms(t["y1"])
        z2_ref[...] = x_ref[...] + n1 * norm_ref[1:2, :] * modx_ref[2:3, :]

    lat = lambda i: (i, 0)
    full = lambda i: (i + n_ctx_tiles, 0)
    fixed = lambda i: (0, 0)

    def pcol(blk):
        return pl.BlockSpec((TM, HW), lambda i: (i + n_ctx_tiles, blk))

    in_specs = ([pl.BlockSpec((TM, d), lat)] + [pl.BlockSpec((TM, HW), full)] * 4
                + [pcol(C_HGATE), pcol(C_GGATE), pcol(9), pcol(10), pcol(11), pcol(12)]
                + [pl.BlockSpec((8, d), fixed)] * 3 + [VMEM_SPEC] * 3)
    bf = lambda w: jax.ShapeDtypeStruct((rows, w), BF16)
    f32 = jax.ShapeDtypeStruct((rows, d), F32)
    return pl.pallas_call(
        body, name=name, grid=(rows // TM,), out_shape=[f32, f32, bf(d), bf(HW), bf(HW)], in_specs=in_specs,
        out_specs=[pl.BlockSpec((TM, d), lat)] * 3 + [pl.BlockSpec((TM, HW), lat)] * 2,
        compiler_params=_cparams(dimension_semantics=("parallel",)),
    )(x_lat, *o_list, p, p, p, p, p, p, modx, norms, onorms, w_br_hg, w_br_gla, w_out)


def _ffn_fwd_bwd(z2, modx, norms, w_gate, w_up, w_down, target, name):
    rows, d = z2.shape
    dff = w_gate.shape[0]
    inv_d = 1.0 / d

    def body(z2_ref, modx_ref, norm_ref, wg_ref, wu_ref, wd_ref, t_ref,
             loss_ref, dz2_ref, h2_ref, a_ref, du_ref, dv_ref, dy2_ref, stat_ref):
        i = pl.program_id(0)
        pre2, post2 = norm_ref[2:3, :], norm_ref[3:4, :]
        shift2, scale2, gate2 = modx_ref[3:4, :], modx_ref[4:5, :], modx_ref[5:6, :]
        z2 = z2_ref[...]
        n2, r2 = _rms(z2)
        nw2 = n2 * pre2
        h2 = (nw2 * (1.0 + scale2) + shift2).astype(BF16)
        h2_ref[...] = h2
        u = _dot_nt(h2, wg_ref[...])
        v = _dot_nt(h2, wu_ref[...])
        su = _silu(u)
        a = (su * v).astype(BF16)
        a_ref[...] = a
        y2 = jnp.dot(a, wd_ref[...], preferred_element_type=F32)
        n3, r3 = _rms(y2)
        err = z2 + n3 * post2 * gate2 - t_ref[...]
        part = 0.5 * inv_d * jnp.sum(err * err)
        dz3 = err * inv_d
        dgate2 = _colsum(dz3 * n3 * post2)
        tt = dz3 * gate2
        dpost2 = _colsum(tt * n3)
        dy2 = _rms_bwd(tt * post2, n3, r3).astype(BF16)
        dy2_ref[...] = dy2
        da = _dot_nt(dy2, wd_ref[...])
        du = (da * v * _dsilu(u)).astype(BF16)
        dv = (da * su).astype(BF16)
        du_ref[...] = du
        dv_ref[...] = dv
        dh2 = (jnp.dot(du, wg_ref[...], preferred_element_type=F32)
               + jnp.dot(dv, wu_ref[...], preferred_element_type=F32))
        dshift2 = _colsum(dh2)
        dscale2 = _colsum(dh2 * nw2)
        dnw2 = dh2 * (1.0 + scale2)
        dpre2 = _colsum(dnw2 * n2)
        dz2_ref[...] = dz3 + _rms_bwd(dnw2 * pre2, n2, r2)

        @pl.when(i == 0)
        def _():
            stat_ref[...] = jnp.zeros_like(stat_ref)
            loss_ref[...] = jnp.zeros_like(loss_ref)

        for r, val in enumerate((dshift2, dscale2, dgate2, dpre2, dpost2)):
            stat_ref[r:r + 1, :] += val
        loss_ref[...] += part
        stat_ref[5:6, 0:128] += part

    lat = lambda i: (i, 0)
    fixed = lambda i: (0, 0)
    bf = lambda w: jax.ShapeDtypeStruct((rows, w), BF16)
    return pl.pallas_call(
        body, name=name, grid=(rows // TM,),
        out_shape=[jax.ShapeDtypeStruct((8, 128), F32), jax.ShapeDtypeStruct((rows, d), F32), bf(d), bf(dff), bf(dff),
                   bf(dff), bf(d), jax.ShapeDtypeStruct((8, d), F32)],
        in_specs=[pl.BlockSpec((TM, d), lat), pl.BlockSpec((8, d), fixed), pl.BlockSpec((8, d), fixed)]
        + [VMEM_SPEC] * 3 + [pl.BlockSpec((TM, d), lat)],
        out_specs=[pl.BlockSpec((8, 128), fixed), pl.BlockSpec((TM, d), lat), pl.BlockSpec((TM, d), lat),
                   pl.BlockSpec((TM, dff), lat), pl.BlockSpec((TM, dff), lat), pl.BlockSpec((TM, dff), lat),
                   pl.BlockSpec((TM, d), lat), pl.BlockSpec((8, d), fixed)],
        compiler_params=_cparams(dimension_semantics=("arbitrary",)),
    )(z2, modx, norms, w_gate, w_up, w_down, target)


def _mixer_tail_bwd(x_lat, p, o_list, dz2, y1, modx, norms, onorms, w_br_hg, w_br_gla, w_out, n_ctx_tiles, n_tiles,
                    name):
    rows, d = x_lat.shape
    total = n_tiles * TM

    def body(x_ref, ofw_hg, obw_hg, ofw_gla, obw_gla, p_hgate, p_ggate, p_ghg_a, p_ghg_b, p_ggla_a, p_ggla_b,
             dz2_ref, y1_ref, modx_ref, norm_ref, on_ref, wbh_ref, wbg_ref, wout_ref,
             dohg_ref, dogla_ref, dhgate_ref, dggate_ref, dghg_ref, dggla_ref, dy1_ref, dbhg_ref, dbgla_ref,
             stat_ref):
        i = pl.program_id(0)

        @pl.when(i == 0)
        def _():
            stat_ref[...] = jnp.zeros_like(stat_ref)

        @pl.when(i < n_ctx_tiles)
        def _():
            for ref in (dohg_ref, dogla_ref, dhgate_ref, dggate_ref, dghg_ref, dggla_ref):
                ref[...] = jnp.zeros_like(ref)

        @pl.when(i >= n_ctx_tiles)
        def _():
            post1, gate1 = norm_ref[1:2, :], modx_ref[2:3, :]
            hg_on, gla_on = on_ref[0:1, 0:HD], on_ref[1:2, 0:HD]
            p_gate_hg = jnp.concatenate([p_ghg_a[...], p_ghg_b[...]], axis=1)
            p_gate_gla = jnp.concatenate([p_ggla_a[...], p_ggla_b[...]], axis=1)
            ph, pg = p_hgate[...], p_ggate[...]
            t = _mixer_tail(x_ref[...], ofw_hg[...] + obw_hg[...], ofw_gla[...] + obw_gla[...], ph, pg,
                            p_gate_hg, p_gate_gla, hg_on, gla_on, wbh_ref[...], wbg_ref[...], wout_ref[...])
            dz2 = dz2_ref[...]
            n1, r1 = _rms(y1_ref[...])
            dgate1 = _colsum(dz2 * n1 * post1)
            tt = dz2 * gate1
            dpost1 = _colsum(tt * n1)
            dy1 = _rms_bwd(tt * post1, n1, r1).astype(BF16)
            dy1_ref[...] = dy1
            dmerged = _dot_nt(dy1, wout_ref[...])
            dghg_ref[...] = (dmerged * t["b_hg"] * t["s_hg"] * (1.0 - t["s_hg"])).astype(BF16)
            dggla_ref[...] = (dmerged * t["b_gla"] * t["s_gla"] * (1.0 - t["s_gla"])).astype(BF16)
            db_hg = (dmerged * t["s_hg"]).astype(BF16)
            db_gla = (dmerged * t["s_gla"]).astype(BF16)
            dbhg_ref[...] = db_hg
            dbgla_ref[...] = db_gla
            don_acc = []
            for (db, wb, pgate, on, ns, rs, gain, gate_ref, do_ref) in (
                    (db_hg, wbh_ref, ph, t["on_hg"], t["n_hg"], t["r_hg"], hg_on, dhgate_ref, dohg_ref),
                    (db_gla, wbg_ref, pg, t["on_gla"], t["n_gla"], t["r_gla"], gla_on, dggate_ref, dogla_ref)):
                dog = _dot_nt(db, wb[...])
                gate_ref[...] = (dog * on * _dsilu(pgate)).astype(BF16)
                don = dog * _silu(pgate)
                acc = jnp.zeros((1, HD), F32)
                for h in range(NH):
                    sl = slice(h * HD, (h + 1) * HD)
                    acc = acc + _colsum(don[:, sl] * ns[h])
                    do_ref[:, sl] = _rms_bwd(don[:, sl] * gain, ns[h], rs[h]).astype(BF16)
                don_acc.append(acc)
            stat_ref[0:1, :] += dgate1
            stat_ref[1:2, :] += dpost1
            stat_ref[2:3, 0:HD] += don_acc[0]
            stat_ref[2:3, HD:2 * HD] += don_acc[1]

    lat = lambda i: (jnp.maximum(i - n_ctx_tiles, 0), 0)
    full = lambda i: (i, 0)
    fixed = lambda i: (0, 0)

    def pcol(blk):
        return pl.BlockSpec((TM, HW), lambda i: (i, blk))

    in_specs = ([pl.BlockSpec((TM, d), lat)] + [pl.BlockSpec((TM, HW), full)] * 4
                + [pcol(C_HGATE), pcol(C_GGATE), pcol(9), pcol(10), pcol(11), pcol(12)]
                + [pl.BlockSpec((TM, d), lat), pl.BlockSpec((TM, d), lat)]
                + [pl.BlockSpec((8, d), fixed)] * 3 + [VMEM_SPEC] * 3)
    f = lambda w: jax.ShapeDtypeStruct((total, w), BF16)
    out_shape = [f(HW), f(HW), f(HW), f(HW), f(d), f(d), jax.ShapeDtypeStruct((rows, d), BF16),
                 jax.ShapeDtypeStruct((rows, d), BF16), jax.ShapeDtypeStruct((rows, d), BF16),
                 jax.ShapeDtypeStruct((8, d), F32)]
    out_specs = ([pl.BlockSpec((TM, HW), full)] * 4 + [pl.BlockSpec((TM, d), full)] * 2
                 + [pl.BlockSpec((TM, d), lat)] * 3 + [pl.BlockSpec((8, d), fixed)])
    return pl.pallas_call(
        body, name=name, grid=(n_tiles,), out_shape=out_shape, in_specs=in_specs, out_specs=out_specs,
        compiler_params=_cparams(dimension_semantics=("arbitrary",)),
    )(x_lat, *o_list, p, p, p, p, p, p, dz2, y1, modx, norms, onorms, w_br_hg, w_br_gla, w_out)


def _in_projection_bwd(ctx0, x0, dz2, modc, modx, pre1, w_t, pieces, n_ctx_tiles, name):
    d = x0.shape[1]
    rows = ctx0.shape[0] + x0.shape[0]
    lat_rows = dz2.shape[0]
    width = P_WIDTH
    n_pieces = len(pieces)

    def body(*refs):
        ctx_ref, x_ref, dz2_ref, modc_ref, modx_ref, pre_ref, w_ref = refs[:7]
        (dhq_f, dhq_b, dhi_f, dhi_b, dhf_f, dhf_b, dhgate, dgq_f, dgq_b, dgk_f, dgk_b, dgv_f, dgv_b, dggate,
         dghg, dggla, dlr_f, dlr_b) = refs[7:7 + n_pieces]
        dp_ref, gx_ref, stat_ref = refs[7 + n_pieces:]
        i = pl.program_id(0)
        is_ctx = i < n_ctx_tiles
        z = jnp.where(is_ctx, ctx_ref[...], x_ref[...])
        sections = [
            (0, dhq_f[...] + dhq_b[...]), (HW, dhi_f[...] + dhi_b[...]), (2 * HW, dhf_f[...]), (3 * HW, dhf_b[...]),
            (4 * HW, dhgate[...]), (5 * HW, dgq_f[...] + dgq_b[...]), (6 * HW, dgk_f[...] + dgk_b[...]),
            (7 * HW, dgv_f[...] + dgv_b[...]), (8 * HW, dggate[...]),
            (9 * HW, dghg[:, 0:HW]), (10 * HW, dghg[:, HW:2 * HW]),
            (11 * HW, dggla[:, 0:HW]), (12 * HW, dggla[:, HW:2 * HW]), (OFF_LR, dlr_f[...] + dlr_b[...])]
        dh = jnp.zeros((TM, d), F32)
        for off, val in sections:
            w = val.shape[1]
            vb = val.astype(BF16)
            dp_ref[off:off + w, :] = vb.T
            dh = dh + jnp.dot(vb, w_ref[_w_in_row(off):_w_in_row(off) + w, :], preferred_element_type=F32)
        n, r = _rms(z)
        pre = pre_ref[...]
        scale = jnp.where(is_ctx, modc_ref[1:2, :], modx_ref[1:2, :])
        nw = n * pre
        dshift = _colsum(dh)
        dscale = _colsum(dh * nw)
        dnw = dh * (1.0 + scale)
        dpre = _colsum(dnw * n)
        gx_ref[...] = dz2_ref[...] + _rms_bwd(dnw * pre, n, r)
        zero = jnp.zeros((1, d), F32)

        @pl.when(i == 0)
        def _():
            stat_ref[...] = jnp.zeros_like(stat_ref)

        stat_ref[0:1, :] += jnp.where(is_ctx, zero, dshift)
        stat_ref[1:2, :] += jnp.where(is_ctx, zero, dscale)
        stat_ref[2:3, :] += jnp.where(is_ctx, dshift, zero)
        stat_ref[3:4, :] += jnp.where(is_ctx, dscale, zero)
        stat_ref[4:5, :] += dpre

    full = lambda i: (i, 0)
    lat = lambda i: (jnp.maximum(i - n_ctx_tiles, 0), 0)
    fixed = lambda i: (0, 0)
    piece_specs = [pl.BlockSpec((TM, a.shape[1]), full) for a in pieces]
    in_specs = [pl.BlockSpec((TM, d), lambda i: (jnp.minimum(i, n_ctx_tiles - 1), 0)), pl.BlockSpec((TM, d), lat),
                pl.BlockSpec((TM, d), lat), pl.BlockSpec((8, d), fixed),
                pl.BlockSpec((8, d), fixed), pl.BlockSpec((1, d), fixed), VMEM_SPEC] + piece_specs
    return pl.pallas_call(
        body, name=name, grid=(rows // TM,),
        out_shape=[jax.ShapeDtypeStruct((width, rows), BF16), jax.ShapeDtypeStruct((lat_rows, d), F32),
                   jax.ShapeDtypeStruct((8, d), F32)],
        in_specs=in_specs,
        out_specs=[pl.BlockSpec((width, TM), lambda i: (0, i)), pl.BlockSpec((TM, d), lat),
                   pl.BlockSpec((8, d), fixed)],
        compiler_params=_cparams(dimension_semantics=("arbitrary",)),
    )(ctx0, x0, dz2, modc, modx, pre1, w_t, *pieces)


def _transposed_lhs_matmul(x_ref, dy_ref, o_ref, xt_ref):
    @pl.when(pl.program_id(1) == 0)
    def _():
        xt_ref[...] = x_ref[...].T

    o_ref[...] = jnp.dot(xt_ref[...], dy_ref[...], preferred_element_type=F32)


def _w_in_grad(dp_t, h1, n_cols, name):
    rows, d = h1.shape
    n_main = OFF_LR // HW
    lr0 = _w_in_row(OFF_LR)

    def body(x_ref, xlr_ref, h_ref, o_hbm, acc_ref, sem):
        i = pl.program_id(0)

        def main_copy(step):
            row = jnp.where(step < 9, step * HW, step * HW + 2 * RANK)
            return pltpu.make_async_copy(acc_ref, o_hbm.at[pl.ds(pl.multiple_of(row, 8), HW), :], sem)

        lr_copy = pltpu.make_async_copy(acc_ref.at[0:2 * RANK, :], o_hbm.at[lr0:lr0 + 2 * RANK, :], sem)

        @pl.when(i > 0)
        def _():
            main_copy(i - 1).wait()

        @pl.when(i < n_main)
        def _():
            acc_ref[...] = jnp.dot(x_ref[...], h_ref[...], preferred_element_type=F32)
            main_copy(i).start()

        @pl.when(i == n_main)
        def _():
            acc_ref[0:128, :] = jnp.dot(xlr_ref[...], h_ref[...], preferred_element_type=F32)
            lr_copy.start()
            lr_copy.wait()

    return pl.pallas_call(
        body, name=name, grid=(n_main + 1,),
        out_shape=jax.ShapeDtypeStruct((n_cols, d), F32),
        in_specs=[pl.BlockSpec((HW, rows), lambda i: (jnp.minimum(i, n_main - 1), 0)),
                  pl.BlockSpec((128, rows), lambda i: (OFF_LR // 128, 0)),
                  pl.BlockSpec((rows, d), lambda i: (0, 0))],
        out_specs=ANY_SPEC,
        scratch_shapes=[pltpu.VMEM((HW, d), F32), pltpu.SemaphoreType.DMA],
        compiler_params=_cparams(dimension_semantics=("arbitrary",)),
    )(dp_t, dp_t, h1)


def _weight_grad(xs, dy, name, tk=None, tn=512, k_first=0, k_tiles=None):
    rows = dy.shape[0]
    n = dy.shape[1]
    tn_ = min(tn, n)
    tk_ = xs.shape[1] if tk is None else tk
    k_tiles = xs.shape[1] // tk_ if k_tiles is None else k_tiles
    k = k_tiles * tk_

    return pl.pallas_call(
        functools.partial(_transposed_lhs_matmul), name=name, grid=(k_tiles, n // tn_),
        out_shape=jax.ShapeDtypeStruct((k, n), F32),
        in_specs=[pl.BlockSpec((rows, tk_), lambda i, j: (0, i + k_first)),
                  pl.BlockSpec((rows, tn_), lambda i, j: (0, j))],
        out_specs=pl.BlockSpec((tk_, tn_), lambda i, j: (i, j)),
        scratch_shapes=[pltpu.VMEM((tk_, rows), BF16)],
        compiler_params=_cparams(dimension_semantics=("parallel", "arbitrary")),
    )(xs, dy)


def _running_sums(xs, fws):
    c = xs[0].shape[0]
    row = lax.broadcasted_iota(jnp.int32, (c, 1), 0)
    s = 1
    while s < c:
        xs = [x + (jnp.where(row >= s, pltpu.roll(x, s, axis=0), 0.0) if fw else
                   jnp.where(row < c - s, pltpu.roll(x, c - s, axis=0), 0.0)) for x, fw in zip(xs, fws)]
        s *= 2
    return xs


def _chunks_terms(qs, ks, gs, fws):
    c = CHUNK
    n = len(qs)
    r = lax.broadcasted_iota(jnp.int32, (c, c), 0)
    s = lax.broadcasted_iota(jnp.int32, (c, c), 1)
    row = lax.broadcasted_iota(jnp.int32, (c, 1), 0)
    per_dir = {}
    for fw in set(fws):
        pos = row if fw else (c - 1 - row)
        per_dir[fw] = dict(
            causal=(s <= r) if fw else (s >= r), causal_t=(s >= r) if fw else (s <= r), pos=pos,
            in_blk=[(pos >= SUB * j) & (pos < SUB * (j + 1)) for j in range(NSUB)],
            start_row=[None] + [SUB * j - 1 if fw else c - SUB * j for j in range(1, NSUB)],
            rend=c - 1 if fw else 0)
    dirs = [per_dir[fw] for fw in fws]
    cums = _running_sums(gs, fws)
    starts = [[None] + [cum[d["start_row"][j]:d["start_row"][j] + 1, :] for j in range(1, NSUB)]
              for cum, d in zip(cums, dirs)]
    es = [[jnp.exp(cum) for cum in cums]]
    for j in range(1, NSUB):
        es.append([jnp.exp(jnp.where(d["pos"] >= SUB * j, cum - st[j], -1e30)) for cum, st, d in zip(cums, starts, dirs)])
    owns = [functools.reduce(lambda rest, j: jnp.where(d["in_blk"][j], st[j], rest), range(1, NSUB), 0.0)
            for st, d in zip(starts, dirs)]
    kscales = [jnp.exp(own - cum) for own, cum in zip(owns, cums)]
    cends = [cum[d["rend"]:d["rend"] + 1, :] for cum, d in zip(cums, dirs)]
    tails = [jnp.exp(cend - cum) for cend, cum in zip(cends, cums)]
    qcats = [jnp.concatenate([q * es[j][i] for j in range(NSUB)], axis=1).astype(BF16) for i, q in enumerate(qs)]
    kts = [k * ksc for k, ksc in zip(ks, kscales)]
    kms = [jnp.concatenate([jnp.where(d["in_blk"][j], kt, 0.0) for j in range(NSUB)], axis=1).astype(BF16)
           for kt, d in zip(kts, dirs)]
    e_by_lane = [[es[j][i] for j in range(NSUB)] for i in range(n)]
    return dict(dirs=dirs, e=e_by_lane, kscale=kscales, cend=cends, tail=tails, qcat=qcats, km=kms, kt=kts)


def _chunks_fwd(qs, ks, vs, gs, st0s, fws):
    t = _chunks_terms(qs, ks, gs, fws)
    scores = [_dot_nt(qc, km) for qc, km in zip(t["qcat"], t["km"])]
    a = [jnp.where(d["causal"], sc, 0.0) for sc, d in zip(scores, t["dirs"])]
    inter = [_dot_nt(qc[:, 0:HD], st0) for qc, st0 in zip(t["qcat"], st0s)]
    intra = [_dot(a_, v) for a_, v in zip(a, vs)]
    os_ = [x + y for x, y in zip(intra, inter)]
    upd = [_dot_tn(v, k * tl) for v, k, tl in zip(vs, ks, t["tail"])]
    st1s = [st0 * jnp.exp(ce) + u for st0, ce, u in zip(st0s, t["cend"], upd)]
    return os_, st1s


def _chunks_bwd(qs, ks, vs, gs, st0s, dos, dst1s, fws):
    n = len(qs)
    t = _chunks_terms(qs, ks, gs, fws)
    qcat, km, e, dirs = t["qcat"], t["km"], t["e"], t["dirs"]
    a_t = [jnp.where(d["causal_t"], _dot_nt(km_, qc), 0.0) for km_, qc, d in zip(km, qcat, dirs)]
    ktail = [k * tl for k, tl in zip(ks, t["tail"])]
    dv_a = [_dot(at, do) for at, do in zip(a_t, dos)]
    dv_b = [_dot_nt(kt, ds) for kt, ds in zip(ktail, dst1s)]
    dv = [x + y for x, y in zip(dv_a, dv_b)]
    da = [jnp.where(d["causal"], _dot_nt(do, v), 0.0) for do, v, d in zip(dos, vs, dirs)]
    da_t = [jnp.where(d["causal_t"], _dot_nt(v, do), 0.0) for do, v, d in zip(dos, vs, dirs)]
    dqcat = [_dot(da_, km_) for da_, km_ in zip(da, km)]
    dq_inter = [e[i][0] * _dot(dos[i], st0s[i]) for i in range(n)]
    dkm = [_dot(dat, qc) for dat, qc in zip(da_t, qcat)]
    dk_inter = [_dot(v, ds) * tl for v, ds, tl in zip(vs, dst1s, t["tail"])]
    dq = [dq_inter[i] + sum(e[i][j] * dqcat[i][:, j * HD:(j + 1) * HD] for j in range(NSUB)) for i in range(n)]
    dkt = [sum(jnp.where(dirs[i]["in_blk"][j], dkm[i][:, j * HD:(j + 1) * HD], 0.0) for j in range(NSUB))
           for i in range(n)]
    dk = [dkt[i] * t["kscale"][i] + dk_inter[i] for i in range(n)]
    dcum = [qs[i] * dq_inter[i] - ks[i] * dk_inter[i] - t["kt"][i].astype(BF16).astype(F32) * dkt[i]
            + sum(qcat[i][:, j * HD:(j + 1) * HD].astype(F32) * dqcat[i][:, j * HD:(j + 1) * HD] for j in range(NSUB))
            for i in range(n)]
    ecend = [jnp.exp(ce) for ce in t["cend"]]
    end = [ecend[i] * _colsum(st0s[i] * dst1s[i]) + _colsum(ks[i] * dk_inter[i]) for i in range(n)]
    sums = _running_sums(dcum, [not fw for fw in fws])
    dg = [sm + en for sm, en in zip(sums, end)]
    upd = [_dot_tn(dos[i], qs[i] * e[i][0]) for i in range(n)]
    dst0 = [dst1s[i] * ecend[i] + upd[i] for i in range(n)]
    return dq, dk, dv, dg, dst0


def _chunk_index(step, n_ctx_chunks, n_chunks, fw):
    if fw:
        return step
    return jnp.where(step < n_ctx_chunks, n_ctx_chunks - 1 - step, n_chunks - 1 + n_ctx_chunks - step)


def _hg_inputs(hq, hf, lbv, d_idx, sl):
    lb = _sigmoid(lbv[d_idx:d_idx + 1, sl] - lbv[2 + d_idx:3 + d_idx, sl])
    sg = _sigmoid(hf)
    f = lb + (1.0 - lb) * sg
    return _silu(hq), 1.0 - f, jnp.log(f), f, sg, lb


def _scan_fwd_both(p, side, n_ctx_chunks, branch, name):
    rows = p.shape[0]
    n_chunks = rows // CHUNK
    hg = branch == "hg"
    n_in = 4 if hg else 6

    def body(*refs):
        ins, outs, state = refs[:2 * n_in], refs[2 * n_in:2 * n_in + 4], refs[-1]

        @pl.when(pl.program_id(0) == 0)
        def _():
            state[...] = jnp.zeros_like(state)

        lanes, where = [], []
        for di, fw in enumerate((True, False)):
            r = ins[di * n_in:(di + 1) * n_in]
            o_ref, st_ref = outs[2 * di], outs[2 * di + 1]
            if hg:
                a_ref, b_ref, c_ref, lb_ref = r
            else:
                a_ref, b_ref, c_ref, lr_ref, wgk_ref, bgk_ref = r
                logits = _dot(lr_ref[...], wgk_ref[...]) + bgk_ref[...]
                g_all = _log_sigmoid(logits) * (1.0 / GATE_NORM)
            for h in range(NH):
                sl = slice(h * HD, (h + 1) * HD)
                if hg:
                    q, k, g, _, _, _ = _hg_inputs(a_ref[:, sl], c_ref[:, sl], lb_ref[...], di, sl)
                    v = b_ref[:, sl]
                else:
                    q, k, v, g = a_ref[:, sl] * (HD ** -0.5), b_ref[:, sl], c_ref[:, sl], g_all[:, sl]
                lanes.append((q, k, v, g, state[di, h], fw))
                where.append((di, h, sl, o_ref, st_ref))
        qs, ks, vs, gs, st0s, fws = (list(col) for col in zip(*lanes))
        os_, st1s = _chunks_fwd(qs, ks, vs, gs, st0s, fws)
        for (di, h, sl, o_ref, st_ref), st0, o, st1 in zip(where, st0s, os_, st1s):
            st_ref[0, h] = st0
            o_ref[:, sl] = o
            state[di, h] = st1

    fixed = lambda j: (0, 0)
    in_specs, args, out_specs = [], [], []
    for di, fw in enumerate((True, False)):
        chunk = functools.partial(_chunk_index, n_ctx_chunks=n_ctx_chunks, n_chunks=n_chunks, fw=fw)

        def cmap(blk, width=HW, chunk=chunk):
            return pl.BlockSpec((CHUNK, width), lambda j: (chunk(j), blk))

        if hg:
            in_specs += [cmap(C_HQ), cmap(C_HI), cmap(C_HF_FW + di), pl.BlockSpec((4, HW), fixed)]
            args += [p, p, p, side]
        else:
            in_specs += [cmap(C_GQ), cmap(C_GK), cmap(C_GV), cmap(OFF_LR // 128, 128),
                         pl.BlockSpec((128, HW), fixed), pl.BlockSpec((1, HW), fixed)]
            args += [p, p, p, p, side[di][0], side[di][1]]
        out_specs += [cmap(0), pl.BlockSpec((1, NH, HD, HD), lambda j, chunk=chunk: (chunk(j), 0, 0, 0))]
    return pl.pallas_call(
        body, name=name, grid=(n_chunks,),
        out_shape=[jax.ShapeDtypeStruct((rows, HW), F32), jax.ShapeDtypeStruct((n_chunks, NH, HD, HD), F32)] * 2,
        in_specs=in_specs, out_specs=out_specs,
        scratch_shapes=[pltpu.VMEM((2, NH, HD, HD), F32)],
        compiler_params=_cparams(dimension_semantics=("arbitrary",)),
    )(*args)


def _scan_bwd_both(p, side, states, d_o, n_ctx_chunks, branch, name):
    rows = p.shape[0]
    n_chunks = rows // CHUNK
    hg = branch == "hg"
    n_in = 6 if hg else 8
    n_out = 4 if hg else 6

    def body(*refs):
        ins, outs, dstate = refs[:2 * n_in], refs[2 * n_in:2 * n_in + 2 * n_out], refs[-1]
        first = pl.program_id(0) == 0

        @pl.when(first)
        def _():
            dstate[...] = jnp.zeros_like(dstate)

        lanes, where, extra, ctx = [], [], [], []
        for di, fw in enumerate((True, False)):
            r, w = ins[di * n_in:(di + 1) * n_in], outs[di * n_out:(di + 1) * n_out]
            if hg:
                a_ref, b_ref, c_ref, lb_ref, st_ref, do_ref = r
                da_ref, db_ref, dc_ref, dlb_ref = w
                acc_refs = (dlb_ref,)
            else:
                a_ref, b_ref, c_ref, lr_ref, wgk_ref, bgk_ref, st_ref, do_ref = r
                da_ref, db_ref, dc_ref, dlr_ref, dwgk_ref, dbias_ref = w
                acc_refs = (dwgk_ref, dbias_ref)
                lr = lr_ref[...]
                logits = _dot(lr, wgk_ref[...]) + bgk_ref[...]
                g_all = _log_sigmoid(logits) * (1.0 / GATE_NORM)

            @pl.when(first)
            def _(acc_refs=acc_refs):
                for ref in acc_refs:
                    ref[...] = jnp.zeros_like(ref)

            for h in range(NH):
                sl = slice(h * HD, (h + 1) * HD)
                if hg:
                    hq, hf = a_ref[:, sl], c_ref[:, sl]
                    q, k, g, f, sg, lb = _hg_inputs(hq, hf, lb_ref[...], di, sl)
                    v = b_ref[:, sl]
                    extra.append((hq, f, sg, lb))
                else:
                    q, k, v, g = a_ref[:, sl] * (HD ** -0.5), b_ref[:, sl], c_ref[:, sl], g_all[:, sl]
                    extra.append(None)
                lanes.append((q, k, v, g, st_ref[0, h], do_ref[:, sl], dstate[di, h], fw))
                where.append((di, h, sl))
            ctx.append((w, None if hg else (lr, logits, wgk_ref)))

        dqs, dks, dvs, dgs, dst0s = [], [], [], [], []
        for lo in range(0, len(lanes), BWD_LANES):
            cols = [list(col) for col in zip(*lanes[lo:lo + BWD_LANES])]
            for acc, part in zip((dqs, dks, dvs, dgs, dst0s), _chunks_bwd(*cols)):
                acc.extend(part)
        dg_parts = {0: [], 1: []}
        for (di, h, sl), ex, dq, dk, dv, dg, dst0 in zip(where, extra, dqs, dks, dvs, dgs, dst0s):
            dstate[di, h] = dst0
            w = ctx[di][0]
            if hg:
                hq, f, sg, lb = ex
                da_ref, db_ref, dc_ref, dlb_ref = w
                da_ref[:, sl] = (dq * _dsilu(hq)).astype(BF16)
                db_ref[:, sl] = dv.astype(BF16)
                df = dg / f - dk
                dc_ref[:, sl] = (df * (1.0 - lb) * sg * (1.0 - sg)).astype(BF16)
                dlb_ref[0:1, sl] += _colsum(df * (1.0 - sg))
            else:
                da_ref, db_ref, dc_ref = w[:3]
                da_ref[:, sl] = (dq * (HD ** -0.5)).astype(BF16)
                db_ref[:, sl] = dk.astype(BF16)
                dc_ref[:, sl] = dv.astype(BF16)
                dg_parts[di].append(dg)
        if not hg:
            for di in range(2):
                dlr_ref, dwgk_ref, dbias_ref = ctx[di][0][3:]
                lr, logits, wgk_ref = ctx[di][1]
                dlogits = jnp.concatenate(dg_parts[di], axis=1) * (1.0 / GATE_NORM) * (1.0 - _sigmoid(logits))
                dlr_ref[...] = _dot_nt(dlogits, wgk_ref[...]).astype(BF16)
                dwgk_ref[...] += _dot_tn(lr, dlogits)
                dbias_ref[0:1, :] += _colsum(dlogits)

    fixed = lambda j: (0, 0)
    big = jax.ShapeDtypeStruct((rows, HW), BF16)
    in_specs, args, out_shape, out_specs = [], [], [], []
    for di, fw in enumerate((True, False)):
        def chunk_of(j, fw=fw):
            return _chunk_index(n_chunks - 1 - j, n_ctx_chunks, n_chunks, fw)

        def cmap(blk, width=HW, chunk_of=chunk_of):
            return pl.BlockSpec((CHUNK, width), lambda j: (chunk_of(j), blk))

        st_spec = pl.BlockSpec((1, NH, HD, HD), lambda j, chunk_of=chunk_of: (chunk_of(j), 0, 0, 0))
        if hg:
            in_specs += [cmap(C_HQ), cmap(C_HI), cmap(C_HF_FW + di), pl.BlockSpec((4, HW), fixed), st_spec, cmap(0)]
            args += [p, p, p, side, states[di], d_o]
            out_shape += [big, big, big, jax.ShapeDtypeStruct((8, HW), F32)]
            out_specs += [cmap(0), cmap(0), cmap(0), pl.BlockSpec((8, HW), fixed)]
        else:
            in_specs += [cmap(C_GQ), cmap(C_GK), cmap(C_GV), cmap(OFF_LR // 128, 128),
                         pl.BlockSpec((128, HW), fixed), pl.BlockSpec((1, HW), fixed), st_spec, cmap(0)]
            args += [p, p, p, p, side[di][0], side[di][1], states[di], d_o]
            out_shape += [big, big, big, jax.ShapeDtypeStruct((rows, 128), BF16),
                          jax.ShapeDtypeStruct((128, HW), F32), jax.ShapeDtypeStruct((8, HW), F32)]
            out_specs += [cmap(0), cmap(0), cmap(0), cmap(0, 128), pl.BlockSpec((128, HW), fixed),
                          pl.BlockSpec((8, HW), fixed)]
    return pl.pallas_call(
        body, name=name, grid=(n_chunks,), out_shape=out_shape, in_specs=in_specs, out_specs=out_specs,
        scratch_shapes=[pltpu.VMEM((2, NH, HD, HD), F32)],
        compiler_params=_cparams(dimension_semantics=("arbitrary",)),
    )(*args)


SMALL_ROWS = 56
ROWS_MOD_X = (0, 1, 8, 16, 17, 18)
ROWS_MOD_C = (2, 3)
ROW_PRE1, ROW_POST1, ROW_ONORM, ROW_PRE2, ROW_POST2, ROW_LB, ROW_BGK, ROW_WGK = 4, 9, 10, 19, 20, 24, 32, 40
ROW_LOSS = 21


def _reduce_small(gathered, lb_full, name):
    _, _, d = gathered.shape

    def body(g_ref, lb_ref, sum_ref, dmod_ref, dbmod_ref, dlb_ref):
        total = g_ref[0]
        for b in range(1, N_DEV):
            total = total + g_ref[b]
        sum_ref[...] = total
        dmod_ref[...] = jnp.zeros_like(dmod_ref)
        for m in range(N_MOD):
            col = slice(m * d, (m + 1) * d)
            acc = jnp.zeros((1, d), F32)
            for b in range(N_DEV):
                row = g_ref[b, ROWS_MOD_X[m]:ROWS_MOD_X[m] + 1, :]
                dmod_ref[b:b + 1, col] = row
                acc = acc + row
            if m < 2:
                ctx_row = total[ROWS_MOD_C[m]:ROWS_MOD_C[m] + 1, :]
                dmod_ref[8:9, col] = ctx_row
                acc = acc + ctx_row
            dbmod_ref[:, col] = acc
        lbv = lb_ref[...]
        for dd in range(2):
            lb = _sigmoid(lbv[dd:dd + 1, :] - lbv[2 + dd:3 + dd, :])
            gl = total[ROW_LB:ROW_LB + 1, dd * HW:(dd + 1) * HW] * lb * (1.0 - lb)
            dlb_ref[dd:dd + 1, :] = gl
            dlb_ref[2 + dd:3 + dd, :] = -gl

    return pl.pallas_call(
        body, name=name,
        out_shape=[jax.ShapeDtypeStruct((SMALL_ROWS, d), F32), jax.ShapeDtypeStruct((16, N_MOD * d), F32),
                   jax.ShapeDtypeStruct((1, N_MOD * d), F32), jax.ShapeDtypeStruct((4, HW), F32)],
        in_specs=[VMEM_SPEC] * 2, out_specs=[VMEM_SPEC] * 4, compiler_params=_cparams(),
    )(gathered, lb_full)


def _c_ctx_grad(gathered, c_ctx_row, name):
    def body(g_ref, c_ref, o_ref):
        acc = g_ref[0, 0:1, :]
        for chip in range(1, N_CHIP):
            acc = acc + g_ref[2 * chip, 0:1, :]
        o_ref[...] = acc * _dsilu(c_ref[...])

    return pl.pallas_call(
        body, name=name, out_shape=jax.ShapeDtypeStruct(c_ctx_row.shape, F32),
        in_specs=[VMEM_SPEC] * 2, out_specs=VMEM_SPEC, compiler_params=_cparams(),
    )(gathered, c_ctx_row)


def _blocked(full, n_blocks):
    k, n = full.shape
    return full.reshape(k, n_blocks, n // n_blocks).transpose(1, 0, 2)


def _unblocked(blocks):
    nb, k, n = blocks.shape
    return blocks.transpose(1, 0, 2).reshape(k, nb * n)


def _sample_front(x0, ctx0, modc, modx, norm_pre1, lb_full, gla_side, w_in_r):
    ctx_len = ctx0.shape[0]
    n_ctx_tiles = ctx_len // TM
    n_ctx_chunks = ctx_len // CHUNK
    h1, p = _in_projection(ctx0, x0, modc, modx, norm_pre1, w_in_r, n_ctx_tiles, "in_projection")
    o_hg_fw, st_hg_fw, o_hg_bw, st_hg_bw = _scan_fwd_both(p, lb_full, n_ctx_chunks, "hg", "scan_hg")
    o_gla_fw, st_gla_fw, o_gla_bw, st_gla_bw = _scan_fwd_both(p, gla_side, n_ctx_chunks, "gla", "scan_gla")
    return dict(h1=h1, p=p, o_list=[o_hg_fw, o_hg_bw, o_gla_fw, o_gla_bw],
                states=[st_hg_fw, st_hg_bw, st_gla_fw, st_gla_bw])


def _sample_back(reduce, front, x0, ctx0, target0, modc, modx, norm_pre1, norms, onorms, lb_full, gla_side, w_in_r,
                 wbh, wbg, wout, ffn_weights):
    seq, d = x0.shape
    ctx_len = ctx0.shape[0]
    n_ctx_tiles = ctx_len // TM
    n_tiles = (ctx_len + seq) // TM
    n_ctx_chunks = ctx_len // CHUNK
    h1, p, o_list = front["h1"], front["p"], front["o_list"]
    st_hg_fw, st_hg_bw, st_gla_fw, st_gla_bw = front["states"]
    z2, y1, merged, og_hg, og_gla = _mixer_tail_fwd(x0, p, o_list, modx, norms, onorms, wbh, wbg, wout, n_ctx_tiles,
                                                    "mixer_tail")
    wg, wu, wd = ffn_weights([z2])
    loss_part, dz2, h2, a_act, du, dv, dy2, stat_ffn = _ffn_fwd_bwd(z2, modx, norms, wg, wu, wd, target0, "ffn")
    dff = wg.shape[0]
    tok = reduce("ffn", [_weight_grad(du, h2, "grad_w_ff_gate", tk=dff // 2, tn=d),
                         _weight_grad(dv, h2, "grad_w_ff_up", tk=dff // 2, tn=d),
                         _weight_grad(a_act, dy2, "grad_w_ff_down", tk=dff // 2)])

    (d_ohg, d_ogla, d_hgate, d_ggate, d_ghg, d_ggla, dy1, db_hg, db_gla, stat_mix) = _mixer_tail_bwd(
        x0, p, o_list, dz2, y1, modx + tok, norms, onorms, wbh, wbg, wout, n_ctx_tiles, n_tiles, "mixer_tail_bwd")
    tok = reduce("mix", [_weight_grad(og_hg, db_hg, "grad_w_br_hg"), _weight_grad(og_gla, db_gla, "grad_w_br_gla"),
                         _weight_grad(merged, dy1, "grad_w_out")])
    tok = tok + reduce("push_ffn", [dy1])
    gla_b = [(wgk, bias + tok) for wgk, bias in gla_side]
    (dgq_f, dgk_f, dgv_f, dlr_f, dwgk_f, dbgk_f, dgq_b, dgk_b, dgv_b, dlr_b, dwgk_b, dbgk_b) = _scan_bwd_both(
        p, gla_b, (st_gla_fw, st_gla_bw), d_ogla, n_ctx_chunks, "gla", "scan_gla_bwd")
    lb_b = lb_full + reduce("push_mix", [dbgk_f])
    (dhq_f, dhi_f, dhf_f, dlb_f, dhq_b, dhi_b, dhf_b, dlb_b) = _scan_bwd_both(
        p, lb_b, (st_hg_fw, st_hg_bw), d_ohg, n_ctx_chunks, "hg", "scan_hg_bwd")
    pieces = [dhq_f, dhq_b, dhi_f, dhi_b, dhf_f, dhf_b, d_hgate, dgq_f, dgq_b, dgk_f, dgk_b, dgv_f, dgv_b, d_ggate,
              d_ghg, d_ggla, dlr_f, dlr_b]
    dp, grad_x, stat_in = _in_projection_bwd(ctx0, x0, dz2, modc, modx, norm_pre1, w_in_r, pieces, n_ctx_tiles,
                                             "in_projection_bwd")

    tok = reduce("in", [_w_in_grad(dp, h1, w_in_r.shape[0], "grad_w_in")])
    reduce("small", dict(stat_in=stat_in + tok, stat_mix=stat_mix, stat_ffn=stat_ffn, dlb=(dlb_f, dlb_b),
                         dwgk=(dwgk_f, dwgk_b), dbgk=(dbgk_f, dbgk_b)))
    reduce("push_in", [])
    return dict(loss_part=loss_part, grad_x=grad_x)


def kernel(x, c, ctx, c_ctx, w_mod, b_mod, norm_pre1, norm_post1, norm_pre2, norm_post2, w_in, hg_lb, hg_onorm, gla_w_gk, gla_b_gk, gla_onorm, w_br_hg, w_br_gla, w_out, w_ff_gate, w_ff_up, w_ff_down, loss_target, m_c_ctx, m_w_mod, m_b_mod, m_norm_pre1, m_norm_post1, m_norm_pre2, m_norm_post2, m_w_in, m_hg_lb, m_hg_onorm, m_gla_w_gk, m_gla_b_gk, m_gla_onorm, m_w_br_hg, m_w_br_gla, m_w_out, m_w_ff_gate, m_w_ff_up, m_w_ff_down, v_c_ctx, v_w_mod, v_b_mod, v_norm_pre1, v_norm_post1, v_norm_pre2, v_norm_post2, v_w_in, v_hg_lb, v_hg_onorm, v_gla_w_gk, v_gla_b_gk, v_gla_onorm, v_w_br_hg, v_w_br_gla, v_w_out, v_w_ff_gate, v_w_ff_up, v_w_ff_down):
    seq, d = x.shape[1], x.shape[2]
    ctx_len = ctx.shape[1]
    assert seq % TM == 0 and ctx_len % TM == 0 and d == 2 * HW
    ax, ay, ac = lax.axis_index("x"), lax.axis_index("y"), lax.axis_index("c")
    chip = 2 * ax + ay
    dev = 2 * chip + ac
    c_arr = jnp.reshape(ac, (1,)).astype(jnp.int32)
    chip_arr = jnp.reshape(chip, (1,)).astype(jnp.int32)
    transposed = ("w_in", "w_ff_gate", "w_ff_up")
    view = lambda a, nm: a[0].T if nm in transposed else a[0]

    sems_in, lands_in, token_in0 = _blocks_start([_cast_into_blocks(chip_arr, view(w_in, "w_in"), "cast_w_in")],
                                                 "gather_w_in_start")

    nc = d // 128
    pad8 = lambda a: jnp.pad(a, ((0, -a.shape[0] % 8), (0, 0)))
    small1 = jnp.concatenate([c.reshape(nc, 128) + token_in0[0, 0], pad8(hg_lb.reshape(4, 128)),
                              gla_w_gk.reshape(2 * RANK, 128), pad8(gla_b_gk.reshape(2, 128))], axis=0)
    blocks = [_cast_into_blocks(chip_arr, view(w_, nm), "cast_" + nm) for w_, nm in (
        (w_br_hg, "w_br_hg"), (w_br_gla, "w_br_gla"), (w_out, "w_out"), (w_ff_gate, "w_ff_gate"),
        (w_ff_up, "w_ff_up"), (w_ff_down, "w_ff_down"))]
    got1 = _allgather8(small1, "gather_small_params", after=blocks)
    c_all = got1[:, :nc, :].reshape(N_DEV, d)
    per_chip = got1[0::2]
    lb_full = per_chip[:, nc:nc + 4, :].transpose(1, 0, 2).reshape(4, HW)
    wgk_full = per_chip[:, nc + 8:nc + 8 + 2 * RANK, :].transpose(1, 0, 2).reshape(2, RANK, HW)
    bgk_full = per_chip[:, nc + 8 + 2 * RANK:nc + 10 + 2 * RANK, :].transpose(1, 0, 2).reshape(2, HW)
    wgk_pad = [jnp.zeros((128, HW), F32).at[dd * RANK:(dd + 1) * RANK].set(wgk_full[dd]) for dd in range(2)]
    bgk = [bgk_full[dd:dd + 1] for dd in range(2)]

    n_mod_cols = w_mod.shape[2]
    cond = jnp.concatenate([c_all, pad8(c_ctx.reshape(1, d))], axis=0)
    b_cols = lax.dynamic_slice(b_mod, (0, chip * n_mod_cols), (1, n_mod_cols))
    lands_in = _blocks_wait(sems_in, lands_in, [got1], "gather_w_in_wait")
    fwd_sems, lands_in, fwd_token = _forward_start(lands_in, "gather_w_in_forward_start")
    mod_part = _mod_forward(cond + fwd_token[0, 0], w_mod[0], b_cols, "mod_forward")
    mod_got = _allgather8(mod_part, "gather_mod")
    mod_all = mod_got[0::2].transpose(1, 0, 2).reshape(16, N_CHIP * n_mod_cols)
    modx = pad8(lax.dynamic_slice(mod_all, (dev, 0), (1, N_MOD * d)).reshape(N_MOD, d))
    modc = pad8(mod_all[8].reshape(N_MOD, d))

    gathered_in = _forward_wait(fwd_sems, lands_in, [mod_got], "gather_w_in_forward_wait")
    sems, lands, token = _blocks_start(blocks, "gather_rest_start", after=[gathered_in[0]])
    w_in_r = gathered_in[0].reshape(-1, d)

    norms = jnp.concatenate([norm_pre1, norm_post1, norm_pre2, norm_post2, jnp.zeros((4, d), F32)], axis=0)
    onorms = jnp.zeros((8, d), F32).at[0, :HD].set(hg_onorm[0]).at[1, :HD].set(gla_onorm[0])
    gla_side = [(wgk_pad[dd], bgk[dd]) for dd in range(2)]
    modx = modx + token[0, 0]
    front = _sample_front(x[0], ctx[0], modc, modx, norm_pre1, lb_full, gla_side, w_in_r)
    lands = _blocks_wait(sems, lands, front["o_list"], "gather_rest_wait")
    gathered = _blocks_finish(lands[:3], "gather_mix_finish")
    wbh, wbg = _unblocked(gathered[0]), _unblocked(gathered[1])
    wout = gathered[2].reshape(d, d)
    ffn_sems, ffn_lands, ffn_token = _forward_start(lands[3:], "gather_ffn_forward_start")
    onorms = onorms + ffn_token[0, 0]

    def ffn_weights(after):
        got = _forward_wait(ffn_sems, ffn_lands, after, "gather_ffn_forward_wait")
        return tuple(g.reshape(-1, d) for g in got)

    dff = w_ff_down.shape[1] * N_CHIP
    groups = {"ffn": ["w_ff_gate", "w_ff_up", "w_ff_down"], "mix": ["w_br_hg", "w_br_gla", "w_out"], "in": ["w_in"]}
    row_sharded = {"w_out": d // N_CHIP, "w_ff_down": dff // N_CHIP, "w_ff_gate": dff // N_CHIP,
                   "w_ff_up": dff // N_CHIP, "w_in": w_in.shape[2]}
    in_flight, to_sibling, small = {}, {}, {}

    def reduce_small(stats):
        small2 = jnp.concatenate([
            stats["stat_in"], stats["stat_mix"], stats["stat_ffn"],
            jnp.concatenate(stats["dlb"], axis=1), jnp.concatenate(stats["dbgk"], axis=1),
            jnp.concatenate([stats["dwgk"][0][0:RANK], stats["dwgk"][1][RANK:2 * RANK]], axis=1)], axis=0)
        assert small2.shape[0] == SMALL_ROWS
        got2 = _allgather8(small2, "gather_small_grads")
        total, dmod_all, g_b_mod, g_lb_full = _reduce_small(got2, lb_full, "reduce_small")
        dmod_cols = lax.dynamic_slice(dmod_all, (0, chip * n_mod_cols), (16, n_mod_cols))
        g_w_mod, cctx_part = _mod_backward(cond, w_mod[0], dmod_cols, "mod_backward")
        got3 = _allgather8(cctx_part, "gather_c_ctx_grad")
        g_c_ctx = _c_ctx_grad(got3, c_ctx.reshape(1, d), "c_ctx_grad")
        small.update(total=total, g_b_mod=g_b_mod, g_lb_full=g_lb_full, g_w_mod=g_w_mod, g_c_ctx=g_c_ctx)

    def reduce(group, grads):
        if group == "small":
            return reduce_small(grads)
        if group.startswith("push_"):
            return push(group[5:], grads)
        nms = groups[group]
        full = [g.reshape(N_CHIP, row_sharded[nm], d) if nm in row_sharded else _blocked(g, N_CHIP)
                for g, nm in zip(grads, nms)]
        sems_, full, lands_, token_ = _send_half_start(full, "grads_to_sibling_start_" + group)
        to_sibling[group] = (sems_, full, lands_)
        return token_[0, 0]

    def push(group, after):
        nms = groups[group]
        sems_, full, lands_ = to_sibling[group]
        if group == "in":
            after = list(after) + [small["g_c_ctx"], small["total"]]
        full, from_sibling = _send_half_wait(sems_, full, lands_, after, "grads_to_sibling_wait_" + group)
        pairs = [_pair_sum(c_arr, f, r_, "pair_sum_" + nm) for f, r_, nm in zip(full, from_sibling, nms)]
        after = [small["g_c_ctx"], small["total"]] if group == "in" else []
        sems_, pairs, lands_, token_ = _scatter_start(pairs, "grads_to_owner_start_" + group, after)
        in_flight[group] = (sems_, pairs, lands_, token_)
        return token_[0, 0]

    r = _sample_back(reduce, front, x[0], ctx[0], loss_target[0], modc, modx, norm_pre1, norms, onorms, lb_full,
                     gla_side, w_in_r, wbh, wbg, wout, ffn_weights)
    grad_x = r["grad_x"]

    weights = dict(w_in=(w_in, m_w_in, v_w_in), w_br_hg=(w_br_hg, m_w_br_hg, v_w_br_hg),
                   w_br_gla=(w_br_gla, m_w_br_gla, v_w_br_gla), w_out=(w_out, m_w_out, v_w_out),
                   w_ff_gate=(w_ff_gate, m_w_ff_gate, v_w_ff_gate), w_ff_up=(w_ff_up, m_w_ff_up, v_w_ff_up),
                   w_ff_down=(w_ff_down, m_w_ff_down, v_w_ff_down))
    names = ["w_in", "w_br_hg", "w_br_gla", "w_out", "w_ff_gate", "w_ff_up", "w_ff_down"]
    big, swapping = {}, {}

    def sum_and_swap(group, after):
        sems_, pairs, lands_, _ = in_flight[group]
        pairs, lands_ = _scatter_wait(sems_, pairs, lands_, after, "grads_to_owner_wait_" + group)
        own_half = [_sum_owner(chip_arr, pr, g, "chip_sum_" + nm) for pr, g, nm in zip(pairs, lands_, groups[group])]
        swapping[group] = _swap_start(own_half, "halves_to_sibling_start_" + group)
        return own_half[-1]

    def update(group, after):
        sems_, own_half, lands_ = swapping[group]
        own_half, other_half = _swap_wait(sems_, own_half, lands_, after, "halves_to_sibling_wait_" + group)
        done = []
        for nm, own, oth in zip(groups[group], own_half, other_half):
            w_, m_, v_ = (view(a, nm) for a in weights[nm])
            res = _adamw_halves(c_arr, own, oth, w_, m_, v_, "adamw_" + nm)
            big[nm] = [r_.T[None] if nm in transposed else r_[None] for r_ in res]
            done.append(res[1])
        return done

    token_in = in_flight["in"][3]
    summed_ffn = sum_and_swap("ffn", [token_in])
    summed_mix = sum_and_swap("mix", [summed_ffn])

    total, g_b_mod, g_lb_full, g_w_mod, g_c_ctx = (small[k] for k in ("total", "g_b_mod", "g_lb_full", "g_w_mod",
                                                                      "g_c_ctx"))
    g_pre1, g_post1, g_pre2, g_post2 = (total[r_:r_ + 1] for r_ in (ROW_PRE1, ROW_POST1, ROW_PRE2, ROW_POST2))
    g_hg_on, g_gla_on = total[ROW_ONORM:ROW_ONORM + 1, 0:HD], total[ROW_ONORM:ROW_ONORM + 1, HD:2 * HD]
    n_lb = hg_lb.shape[2]
    g_hg_lb = lax.dynamic_slice(g_lb_full, (0, chip * n_lb), (4, n_lb))
    g_bgk = lax.dynamic_slice(total[ROW_BGK:ROW_BGK + 1].reshape(2, HW), (0, chip * n_lb), (2, n_lb))
    g_wgk_full = total[ROW_WGK:ROW_WGK + RANK].reshape(RANK, 2, HW).transpose(1, 0, 2).reshape(2 * RANK, HW)
    g_wgk = lax.dynamic_slice(g_wgk_full, (0, chip * n_lb), (2 * RANK, n_lb))

    small_items = [
        (g_c_ctx, c_ctx.reshape(1, d), m_c_ctx.reshape(1, d), v_c_ctx.reshape(1, d)),
        (g_b_mod, b_mod, m_b_mod, v_b_mod),
        (g_pre1, norm_pre1, m_norm_pre1, v_norm_pre1),
        (g_post1, norm_post1, m_norm_post1, v_norm_post1),
        (g_pre2, norm_pre2, m_norm_pre2, v_norm_pre2),
        (g_post2, norm_post2, m_norm_post2, v_norm_post2),
        (g_hg_lb, hg_lb.reshape(4, n_lb), m_hg_lb.reshape(4, n_lb), v_hg_lb.reshape(4, n_lb)),
        (g_hg_on, hg_onorm, m_hg_onorm, v_hg_onorm),
        (g_wgk, gla_w_gk.reshape(2 * RANK, n_lb), m_gla_w_gk.reshape(2 * RANK, n_lb), v_gla_w_gk.reshape(2 * RANK, n_lb)),
        (g_bgk, gla_b_gk.reshape(2, n_lb), m_gla_b_gk.reshape(2, n_lb), v_gla_b_gk.reshape(2, n_lb)),
        (g_gla_on, gla_onorm, m_gla_onorm, v_gla_onorm),
    ]
    small_res = _adamw_whole(small_items, "adamw_small")
    mod_res = _adamw_tiled(g_w_mod, w_mod[0], m_w_mod[0], v_w_mod[0], "adamw_w_mod")
    done_ffn = update("ffn", [summed_mix, mod_res[0], small_res[0][0]])
    done_mix = update("mix", done_ffn)
    update("in", [sum_and_swap("in", done_mix)])

    loss = total[ROW_LOSS, 0]

    shapes = dict(c_ctx=c_ctx.shape, b_mod=b_mod.shape, norm_pre1=norm_pre1.shape, norm_post1=norm_post1.shape,
                  norm_pre2=norm_pre2.shape, norm_post2=norm_post2.shape, hg_lb=hg_lb.shape, hg_onorm=hg_onorm.shape,
                  gla_w_gk=gla_w_gk.shape, gla_b_gk=gla_b_gk.shape, gla_onorm=gla_onorm.shape)
    small_names = ["c_ctx", "b_mod", "norm_pre1", "norm_post1", "norm_pre2", "norm_post2", "hg_lb", "hg_onorm",
                   "gla_w_gk", "gla_b_gk", "gla_onorm"]
    grads, deltas, new_m, new_v = {}, {}, {}, {}
    for nm, item, res in zip(small_names, small_items, small_res):
        grads[nm] = item[0].reshape(shapes[nm])
        deltas[nm], new_m[nm], new_v[nm] = (r_.reshape(shapes[nm]) for r_ in res)
    grads["w_mod"] = g_w_mod[None]
    deltas["w_mod"], new_m["w_mod"], new_v["w_mod"] = (r_[None] for r_ in mod_res)
    for nm in names:
        grads[nm], deltas[nm], new_m[nm], new_v[nm] = big[nm]
    order = ["c_ctx", "w_mod", "b_mod", "norm_pre1", "norm_post1", "norm_pre2", "norm_post2", "w_in", "hg_lb",
             "hg_onorm", "gla_w_gk", "gla_b_gk", "gla_onorm", "w_br_hg", "w_br_gla", "w_out", "w_ff_gate", "w_ff_up",
             "w_ff_down"]
    return (loss, grad_x[None], *[grads[n] for n in order], *[deltas[n] for n in order],
            *[new_m[n] for n in order], *[new_v[n] for n in order])
```

```python
import functools

import jax
import jax.numpy as jnp
from jax import lax
from jax.experimental import pallas as pl
from jax.experimental.pallas import tpu as pltpu

F32 = jnp.float32
BF16 = jnp.bfloat16
MESH = pl.DeviceIdType.MESH

EPS = 1e-6
CHUNK = 64
SUB = 16
NSUB = CHUNK // SUB
NH = 4
HD = 128
HW = NH * HD
RANK = 16
GATE_NORM = 16.0
N_MOD = 6
TM = 256
BWD_LANES = 8
N_DEV = 8
N_CHIP = 4
VMEM_LIMIT = 56 * 1024 * 1024

ADAM_LR = 0.001
ADAM_B1 = 0.9
ADAM_B2 = 0.999
ADAM_EPS = 1e-08
ADAM_WD = 0.01
ADAM_STEP = 10

VMEM_SPEC = pl.BlockSpec(memory_space=pltpu.VMEM)
ANY_SPEC = pl.BlockSpec(memory_space=pl.ANY)
HBM_SPEC = pl.BlockSpec(memory_space=pltpu.HBM)
SEM_SPEC = pl.BlockSpec(memory_space=pltpu.SEMAPHORE)
EFFECT = pltpu.SideEffectType.DATAFLOW_SIDE_EFFECTING


def _cparams(**kw):
    return pltpu.CompilerParams(vmem_limit_bytes=VMEM_LIMIT, **kw)


def _dot(a, b):
    return jnp.dot(a.astype(BF16), b.astype(BF16), preferred_element_type=F32)


def _dot_nt(a, b):
    return lax.dot_general(a.astype(BF16), b.astype(BF16), (((1,), (1,)), ((), ())), preferred_element_type=F32)


def _dot_tn(a, b):
    return lax.dot_general(a.astype(BF16), b.astype(BF16), (((0,), (0,)), ((), ())), preferred_element_type=F32)


def _sigmoid(x):
    return 1.0 / (1.0 + jnp.exp(-x))


def _silu(x):
    return x * _sigmoid(x)


def _dsilu(x):
    s = _sigmoid(x)
    return s * (1.0 + x * (1.0 - s))


def _log_sigmoid(x):
    return jnp.minimum(x, 0.0) - jnp.log(1.0 + jnp.exp(-jnp.abs(x)))


def _colsum(a):
    return jnp.sum(a, axis=0, keepdims=True)


def _rms(a):
    r = lax.rsqrt(jnp.mean(a * a, axis=-1, keepdims=True) + EPS)
    return a * r, r


def _rms_bwd(dn, n, r):
    return r * (dn - n * jnp.mean(dn * n, axis=-1, keepdims=True))


def _place():
    x, y, c = lax.axis_index("x"), lax.axis_index("y"), lax.axis_index("c")
    chips = [(1 - x, y), (x, 1 - y), (1 - x, 1 - y)]
    return x, y, c, chips


def _allgather8(v, name, after=()):
    rows, cols = v.shape
    n_after = len(after)

    def body(x_ref, *rest):
        out_ref, send_sems, recv_sems, local_sem = rest[n_after:]
        x, y, c, chips = _place()
        me, sibling = (x, y, c), (x, y, 1 - c)

        def blk(px, py, pc):
            return out_ref.at[4 * px + 2 * py + pc]

        def copy(k, block, to, src=None):
            return pltpu.make_async_remote_copy(
                src_ref=blk(*block) if src is None else src, dst_ref=blk(*block),
                send_sem=send_sems.at[k], recv_sem=recv_sems.at[k], device_id=to, device_id_type=MESH)

        mine = pltpu.make_async_copy(x_ref, blk(*me), local_sem)
        mine.start()
        first = [copy(0, me, sibling, src=x_ref)]
        first += [copy(1 + j, me, (*chip, c), src=x_ref) for j, chip in enumerate(chips)]
        for cp in first:
            cp.start()
        passed = [copy(4 + j, (*chip, c), sibling) for j, chip in enumerate(chips)]
        for j, chip in enumerate(chips):
            copy(1 + j, (*chip, c), me).wait_recv()
            passed[j].start()
        copy(0, sibling, me).wait_recv()
        for j, chip in enumerate(chips):
            copy(4 + j, (*chip, 1 - c), me).wait_recv()
        for cp in first + passed:
            cp.wait_send()
        mine.wait()

    return pl.pallas_call(
        body, name=name,
        out_shape=jax.ShapeDtypeStruct((N_DEV, rows, cols), v.dtype),
        in_specs=[VMEM_SPEC] + [ANY_SPEC] * n_after, out_specs=VMEM_SPEC,
        scratch_shapes=[pltpu.SemaphoreType.DMA((7,)), pltpu.SemaphoreType.DMA((7,)), pltpu.SemaphoreType.DMA],
    )(v, *after)


def _cast_into_blocks(chip_arr, w, name):
    rows, cols = w.shape
    tr = _row_tile(rows, 16, 256)

    def body(chip_ref, w_ref, o_ref):
        o_ref[0] = w_ref[...].astype(BF16)

    return pl.pallas_call(
        body, name=name,
        grid_spec=pltpu.PrefetchScalarGridSpec(
            num_scalar_prefetch=1, grid=(rows // tr,),
            in_specs=[pl.BlockSpec((tr, cols), lambda i, chip_ref: (i, 0))],
            out_specs=pl.BlockSpec((1, tr, cols), lambda i, chip_ref: (chip_ref[0], i, 0))),
        out_shape=jax.ShapeDtypeStruct((N_CHIP, rows, cols), BF16),
        compiler_params=_cparams(dimension_semantics=("parallel",)),
    )(chip_arr, w)


def _halved_by_rows(shape):
    return (shape[1] // 2) % 16 == 0


def _half_of(ref, pc, block=None):
    lead = slice(None) if block is None else block
    if _halved_by_rows(ref.shape):
        h = ref.shape[1] // 2
        return ref.at[lead, pl.ds(pl.multiple_of(pc * h, 16), h), :]
    h = ref.shape[2] // 2
    return ref.at[lead, :, pl.ds(pl.multiple_of(pc * h, 128), h)]


def _half_shape(shape):
    return (shape[0], shape[1] // 2, shape[2]) if _halved_by_rows(shape) else (shape[0], shape[1], shape[2] // 2)


def _half_rows(ref, chip_id, pc):
    return _half_of(ref, pc, chip_id)


def _hbm(a):
    return pltpu.with_memory_space_constraint(a, pltpu.HBM)


def _blocks_start(lands, name, after=()):
    n = len(lands)
    n_sem = 3 * n
    first = n + len(after)

    def body(*refs):
        lnd = refs[:n]
        send_sems, recv_sems = refs[first:first + n_sem], refs[first + n_sem:first + 2 * n_sem]
        token = refs[-1]
        x, y, c, chips = _place()
        me_chip = 2 * x + y
        for k in range(n):
            for j, chip in enumerate(chips):
                pltpu.make_async_remote_copy(
                    src_ref=_half_rows(lnd[k], me_chip, c), dst_ref=_half_rows(lnd[k], me_chip, c),
                    send_sem=send_sems[3 * k + j], recv_sem=recv_sems[3 * k + j],
                    device_id=(*chip, c), device_id_type=MESH).start()
        token[...] = jnp.zeros_like(token)

    out = pl.pallas_call(
        body, name=name,
        out_shape=(*[pltpu.SemaphoreType.DMA(())] * (2 * n_sem),
                   *[pltpu.HBM(l.shape, l.dtype) for l in lands],
                   jax.ShapeDtypeStruct((8, 128), F32)),
        in_specs=[HBM_SPEC] * n + [ANY_SPEC] * len(after),
        out_specs=(*[SEM_SPEC] * (2 * n_sem), *[HBM_SPEC] * n, VMEM_SPEC),
        input_output_aliases={i: 2 * n_sem + i for i in range(n)},
        compiler_params=pltpu.CompilerParams(has_side_effects=EFFECT),
    )(*[_hbm(l) for l in lands], *after)
    return list(out[:2 * n_sem]), list(out[2 * n_sem:2 * n_sem + n]), out[-1]


def _blocks_wait(sems, lands, after, name):
    n = len(lands)
    n_sem = 3 * n

    def body(*refs):
        lnd = refs[:n]
        s_sems, r_sems = refs[n:n + n_sem], refs[n + n_sem:n + 2 * n_sem]
        x, y, c, chips = _place()
        me_chip = 2 * x + y
        for k in range(n):
            for j, (px, py) in enumerate(chips):
                cp = pltpu.make_async_remote_copy(
                    src_ref=_half_rows(lnd[k], me_chip, c), dst_ref=_half_rows(lnd[k], 2 * px + py, c),
                    send_sem=s_sems[3 * k + j], recv_sem=r_sems[3 * k + j],
                    device_id=(px, py, c), device_id_type=MESH)
                cp.wait_send()
                cp.wait_recv()

    out = pl.pallas_call(
        body, name=name,
        out_shape=tuple(pltpu.HBM(l.shape, l.dtype) for l in lands),
        in_specs=[HBM_SPEC] * n + [SEM_SPEC] * (2 * n_sem) + [ANY_SPEC] * len(after),
        out_specs=[HBM_SPEC] * n,
        input_output_aliases={i: i for i in range(n)},
        compiler_params=pltpu.CompilerParams(has_side_effects=EFFECT),
    )(*lands, *sems, *after)
    return list(out)


def _forward_start(lands, name):
    n = len(lands)
    n_sem = 3 * n

    def body(*refs):
        lnd = refs[:n]
        send_sems, recv_sems = refs[n:n + n_sem], refs[n + n_sem:n + 2 * n_sem]
        x, y, c, chips = _place()
        for k in range(n):
            for j, (px, py) in enumerate(chips):
                pltpu.make_async_remote_copy(
                    src_ref=_half_rows(lnd[k], 2 * px + py, c), dst_ref=_half_rows(lnd[k], 2 * px + py, c),
                    send_sem=send_sems[3 * k + j], recv_sem=recv_sems[3 * k + j],
                    device_id=(x, y, 1 - c), device_id_type=MESH).start()
        refs[-1][...] = jnp.zeros_like(refs[-1])

    out = pl.pallas_call(
        body, name=name,
        out_shape=(*[pltpu.SemaphoreType.DMA(())] * (2 * n_sem), *[pltpu.HBM(l.shape, l.dtype) for l in lands],
                   jax.ShapeDtypeStruct((8, 128), F32)),
        in_specs=[HBM_SPEC] * n,
        out_specs=(*[SEM_SPEC] * (2 * n_sem), *[HBM_SPEC] * n, VMEM_SPEC),
        input_output_aliases={i: 2 * n_sem + i for i in range(n)},
        compiler_params=pltpu.CompilerParams(has_side_effects=EFFECT),
    )(*[_hbm(l) for l in lands])
    return list(out[:2 * n_sem]), list(out[2 * n_sem:2 * n_sem + n]), out[-1]


def _forward_wait(sems, lands, after, name):
    n = len(lands)
    n_sem = 3 * n

    def body(*refs):
        lnd = refs[:n]
        s_sems, r_sems = refs[n:n + n_sem], refs[n + n_sem:n + 2 * n_sem]
        x, y, c, chips = _place()
        for k in range(n):
            for j, (px, py) in enumerate(chips):
                cp = pltpu.make_async_remote_copy(
                    src_ref=_half_rows(lnd[k], 2 * px + py, c), dst_ref=_half_rows(lnd[k], 2 * px + py, 1 - c),
                    send_sem=s_sems[3 * k + j], recv_sem=r_sems[3 * k + j],
                    device_id=(x, y, 1 - c), device_id_type=MESH)
                cp.wait_send()
                cp.wait_recv()

    out = pl.pallas_call(
        body, name=name,
        out_shape=tuple(pltpu.HBM(l.shape, l.dtype) for l in lands),
        in_specs=[HBM_SPEC] * n + [SEM_SPEC] * (2 * n_sem) + [ANY_SPEC] * len(after),
        out_specs=[HBM_SPEC] * n,
        input_output_aliases={i: i for i in range(n)},
        compiler_params=pltpu.CompilerParams(has_side_effects=EFFECT),
    )(*lands, *sems, *after)
    return list(out)


def _blocks_finish(lands, name):
    n = len(lands)

    def body(*refs):
        lnd = refs[n:2 * n]
        send_sems, recv_sems = refs[2 * n:]
        x, y, c, chips = _place()
        sibling = (x, y, 1 - c)

        def copy(k, j, chip_id, pc):
            return pltpu.make_async_remote_copy(
                src_ref=_half_rows(lnd[k], chip_id, pc), dst_ref=_half_rows(lnd[k], chip_id, pc),
                send_sem=send_sems.at[k, j], recv_sem=recv_sems.at[k, j], device_id=sibling, device_id_type=MESH)

        started = []
        for k in range(n):
            for j, (px, py) in enumerate(chips):
                cp = copy(k, j, 2 * px + py, c)
                cp.start()
                started.append(cp)
        for k in range(n):
            for j, (px, py) in enumerate(chips):
                copy(k, j, 2 * px + py, 1 - c).wait_recv()
        for cp in started:
            cp.wait_send()

    out = pl.pallas_call(
        body, name=name,
        out_shape=[jax.ShapeDtypeStruct(l.shape, l.dtype) for l in lands],
        in_specs=[ANY_SPEC] * n, out_specs=[ANY_SPEC] * n,
        input_output_aliases={i: i for i in range(n)},
        scratch_shapes=[pltpu.SemaphoreType.DMA((n, 3)), pltpu.SemaphoreType.DMA((n, 3))],
    )(*lands)
    return list(out)


def _send_half_start(arrs, name):
    n = len(arrs)

    def body(*refs):
        ins, lnd = refs[:n], refs[n:2 * n]
        send_sems, recv_sems = refs[2 * n:3 * n], refs[3 * n:4 * n]
        token = refs[-1]
        x, y, c, _ = _place()
        for k in range(n):
            pltpu.make_async_remote_copy(
                src_ref=_half_of(ins[k], 1 - c), dst_ref=lnd[k], send_sem=send_sems[k], recv_sem=recv_sems[k],
                device_id=(x, y, 1 - c), device_id_type=MESH).start()
        token[...] = jnp.zeros_like(token)

    lands = [_hbm(lax.empty(_half_shape(a.shape), a.dtype)) for a in arrs]
    out = pl.pallas_call(
        body, name=name,
        out_shape=(*[pltpu.SemaphoreType.DMA(())] * (2 * n), *[pltpu.HBM(a.shape, a.dtype) for a in arrs],
                   *[pltpu.HBM(l.shape, l.dtype) for l in lands], jax.ShapeDtypeStruct((8, 128), F32)),
        in_specs=[HBM_SPEC] * (2 * n),
        out_specs=(*[SEM_SPEC] * (2 * n), *[HBM_SPEC] * (2 * n), VMEM_SPEC),
        input_output_aliases={i: 2 * n + i for i in range(2 * n)},
        compiler_params=pltpu.CompilerParams(has_side_effects=EFFECT),
    )(*[_hbm(a) for a in arrs], *lands)
    return list(out[:2 * n]), list(out[2 * n:3 * n]), list(out[3 * n:4 * n]), out[-1]


def _send_half_wait(sems, arrs, lands, after, name):
    n = len(arrs)

    def body(*refs):
        ins, lnd = refs[:n], refs[n:2 * n]
        s_sems, r_sems = refs[2 * n:3 * n], refs[3 * n:4 * n]
        x, y, c, _ = _place()
        for k in range(n):
            cp = pltpu.make_async_remote_copy(
                src_ref=_half_of(ins[k], 1 - c), dst_ref=lnd[k], send_sem=s_sems[k], recv_sem=r_sems[k],
                device_id=(x, y, 1 - c), device_id_type=MESH)
            cp.wait_send()
            cp.wait_recv()

    out = pl.pallas_call(
        body, name=name,
        out_shape=tuple(pltpu.HBM(a.shape, a.dtype) for a in list(arrs) + list(lands)),
        in_specs=[HBM_SPEC] * (2 * n) + [SEM_SPEC] * (2 * n) + [ANY_SPEC] * len(after),
        out_specs=[HBM_SPEC] * (2 * n),
        input_output_aliases={i: i for i in range(2 * n)},
        compiler_params=pltpu.CompilerParams(has_side_effects=EFFECT),
    )(*arrs, *lands, *sems, *after)
    return list(out[:n]), list(out[n:])


def _scatter_start(arrs, name, after=()):
    n = len(arrs)
    n_sem = 3 * n
    first = 2 * n + len(after)

    def body(*refs):
        ins, lnd = refs[:n], refs[n:2 * n]
        send_sems, recv_sems = refs[first:first + n_sem], refs[first + n_sem:first + 2 * n_sem]
        token = refs[-1]
        x, y, c, chips = _place()
        me_chip = 2 * x + y
        for k in range(n):
            for j, (px, py) in enumerate(chips):
                pltpu.make_async_remote_copy(
                    src_ref=ins[k].at[2 * px + py], dst_ref=lnd[k].at[me_chip],
                    send_sem=send_sems[3 * k + j], recv_sem=recv_sems[3 * k + j],
                    device_id=(px, py, c), device_id_type=MESH).start()
        token[...] = jnp.zeros_like(token)

    lands = [_hbm(lax.empty(a.shape, a.dtype)) for a in arrs]
    out = pl.pallas_call(
        body, name=name,
        out_shape=(*[pltpu.SemaphoreType.DMA(())] * (2 * n_sem),
                   *[pltpu.HBM(a.shape, a.dtype) for a in arrs], *[pltpu.HBM(a.shape, a.dtype) for a in arrs],
                   jax.ShapeDtypeStruct((8, 128), F32)),
        in_specs=[HBM_SPEC] * (2 * n) + [ANY_SPEC] * len(after),
        out_specs=(*[SEM_SPEC] * (2 * n_sem), *[HBM_SPEC] * (2 * n), VMEM_SPEC),
        input_output_aliases={i: 2 * n_sem + i for i in range(2 * n)},
        compiler_params=pltpu.CompilerParams(has_side_effects=EFFECT),
    )(*[_hbm(a) for a in arrs], *lands, *after)
    base = 2 * n_sem
    return list(out[:base]), list(out[base:base + n]), list(out[base + n:base + 2 * n]), out[-1]


def _scatter_wait(sems, arrs, lands, after, name):
    n = len(arrs)
    n_sem = 3 * n

    def body(*refs):
        ins, lnd = refs[:n], refs[n:2 * n]
        s_sems, r_sems = refs[2 * n:2 * n + n_sem], refs[2 * n + n_sem:2 * n + 2 * n_sem]
        x, y, c, chips = _place()
        for k in range(n):
            for j, (px, py) in enumerate(chips):
                cp = pltpu.make_async_remote_copy(
                    src_ref=ins[k].at[2 * px + py], dst_ref=lnd[k].at[2 * px + py],
                    send_sem=s_sems[3 * k + j], recv_sem=r_sems[3 * k + j],
                    device_id=(px, py, c), device_id_type=MESH)
                cp.wait_send()
                cp.wait_recv()

    out = pl.pallas_call(
        body, name=name,
        out_shape=tuple(pltpu.HBM(a.shape, a.dtype) for a in list(arrs) + list(lands)),
        in_specs=[HBM_SPEC] * (2 * n) + [SEM_SPEC] * (2 * n_sem) + [ANY_SPEC] * len(after),
        out_specs=[HBM_SPEC] * (2 * n),
        input_output_aliases={i: i for i in range(2 * n)},
        compiler_params=pltpu.CompilerParams(has_side_effects=EFFECT),
    )(*arrs, *lands, *sems, *after)
    return list(out[:n]), list(out[n:])


def _sum_owner(chip_arr, pairs, got, name):
    nb, h, cols = got.shape
    tr = _row_tile(h, 16, 256)

    def body(chip_ref, own_ref, a_ref, b_ref, c_ref, o_ref):
        o_ref[...] = ((own_ref[0].astype(F32) + a_ref[0].astype(F32)) + b_ref[0].astype(F32)) + c_ref[0].astype(F32)

    def slot(off):
        return pl.BlockSpec((1, tr, cols), lambda i, chip_ref: ((chip_ref[0] + off) % N_CHIP, i, 0))

    return pl.pallas_call(
        body, name=name,
        grid_spec=pltpu.PrefetchScalarGridSpec(
            num_scalar_prefetch=1, grid=(h // tr,),
            in_specs=[slot(0), slot(1), slot(2), slot(3)],
            out_specs=pl.BlockSpec((tr, cols), lambda i, chip_ref: (i, 0))),
        out_shape=jax.ShapeDtypeStruct((h, cols), F32),
        compiler_params=_cparams(dimension_semantics=("parallel",)),
    )(chip_arr, pairs, got, got, got)


def _swap_start(arrs, name, after=()):
    n = len(arrs)
    first = 2 * n + len(after)

    def body(*refs):
        ins, lnd = refs[:n], refs[n:2 * n]
        send_sems, recv_sems = refs[first:first + n], refs[first + n:first + 2 * n]
        x, y, c, _ = _place()
        for k in range(n):
            pltpu.make_async_remote_copy(
                src_ref=ins[k], dst_ref=lnd[k], send_sem=send_sems[k], recv_sem=recv_sems[k],
                device_id=(x, y, 1 - c), device_id_type=MESH).start()

    lands = [_hbm(lax.empty(a.shape, a.dtype)) for a in arrs]
    out = pl.pallas_call(
        body, name=name,
        out_shape=(*[pltpu.SemaphoreType.DMA(())] * (2 * n), *[pltpu.HBM(a.shape, a.dtype) for a in arrs],
                   *[pltpu.HBM(a.shape, a.dtype) for a in arrs]),
        in_specs=[HBM_SPEC] * (2 * n) + [ANY_SPEC] * len(after),
        out_specs=(*[SEM_SPEC] * (2 * n), *[HBM_SPEC] * (2 * n)),
        input_output_aliases={i: 2 * n + i for i in range(2 * n)},
        compiler_params=pltpu.CompilerParams(has_side_effects=EFFECT),
    )(*[_hbm(a) for a in arrs], *lands, *after)
    return list(out[:2 * n]), list(out[2 * n:3 * n]), list(out[3 * n:4 * n])


def _swap_wait(sems, arrs, lands, after, name):
    n = len(arrs)

    def body(*refs):
        ins, lnd = refs[:n], refs[n:2 * n]
        s_sems, r_sems = refs[2 * n:3 * n], refs[3 * n:4 * n]
        x, y, c, _ = _place()
        for k in range(n):
            cp = pltpu.make_async_remote_copy(
                src_ref=ins[k], dst_ref=lnd[k], send_sem=s_sems[k], recv_sem=r_sems[k],
                device_id=(x, y, 1 - c), device_id_type=MESH)
            cp.wait_send()
            cp.wait_recv()

    out = pl.pallas_call(
        body, name=name,
        out_shape=tuple(pltpu.HBM(a.shape, a.dtype) for a in list(arrs) + list(lands)),
        in_specs=[HBM_SPEC] * (2 * n) + [SEM_SPEC] * (2 * n) + [ANY_SPEC] * len(after),
        out_specs=[HBM_SPEC] * (2 * n),
        input_output_aliases={i: i for i in range(2 * n)},
        compiler_params=pltpu.CompilerParams(has_side_effects=EFFECT),
    )(*arrs, *lands, *sems, *after)
    return list(out[:n]), list(out[n:])


def _row_tile(h, mult=8, cap=128):
    for t in range(cap - cap % mult, mult - 1, -mult):
        if h % t == 0:
            return t
    if mult > 8:
        return _row_tile(h, 8, cap)
    raise ValueError(h)


def _pair_sum(c_arr, full, recv, name):
    nb, rows, cols = full.shape

    def body(c_ref, f_ref, r_ref, o_ref):
        o_ref[...] = (f_ref[...] + r_ref[...]).astype(BF16)

    if _halved_by_rows(full.shape):
        h = rows // 2
        tr = _row_tile(h, 16, 256)
        steps = h // tr
        own = pl.BlockSpec((1, tr, cols), lambda b, i, c_ref: (b, c_ref[0] * steps + i, 0))
        half = pl.BlockSpec((1, tr, cols), lambda b, i, c_ref: (b, i, 0))
    else:
        steps = 1
        own = pl.BlockSpec((1, rows, cols // 2), lambda b, i, c_ref: (b, 0, c_ref[0]))
        half = pl.BlockSpec((1, rows, cols // 2), lambda b, i, c_ref: (b, 0, 0))
    return pl.pallas_call(
        body, name=name,
        grid_spec=pltpu.PrefetchScalarGridSpec(
            num_scalar_prefetch=1, grid=(nb, steps), in_specs=[own, half], out_specs=half),
        out_shape=jax.ShapeDtypeStruct(_half_shape(full.shape), BF16),
        compiler_params=_cparams(dimension_semantics=("parallel", "parallel")),
    )(c_arr, full, recv)


def _adam_math(g, w, m, v):
    m1 = ADAM_B1 * m + (1.0 - ADAM_B1) * g
    v1 = ADAM_B2 * v + (1.0 - ADAM_B2) * (g * g)
    m_hat = m1 / (1.0 - ADAM_B1 ** ADAM_STEP)
    v_hat = v1 / (1.0 - ADAM_B2 ** ADAM_STEP)
    delta = -ADAM_LR * (m_hat / (jnp.sqrt(v_hat) + ADAM_EPS) + ADAM_WD * w)
    return delta, m1, v1


def _adamw_halves(c_arr, own, other, w, m, v, name):
    rows, cols = w.shape
    by_rows = own.shape[1] == cols

    def body(c_ref, own_ref, oth_ref, w_ref, m_ref, v_ref, g_out, d_out, m_out, v_out):
        if by_rows:
            g = jnp.where(pl.program_id(0) == c_ref[0], own_ref[...], oth_ref[...])
        else:
            own_, oth_ = own_ref[...], oth_ref[...]
            g = jnp.where(c_ref[0] == 0, jnp.concatenate([own_, oth_], axis=1), jnp.concatenate([oth_, own_], axis=1))
        d, m1, v1 = _adam_math(g, w_ref[...], m_ref[...], v_ref[...])
        g_out[...] = g
        d_out[...] = d
        m_out[...] = m1
        v_out[...] = v1

    if by_rows:
        h = rows // 2
        tr = _row_tile(h)
        steps = h // tr
        grid = (2, steps)
        half_spec = pl.BlockSpec((tr, cols), lambda p, i, c_ref: (i, 0))
        full_spec = pl.BlockSpec((tr, cols), lambda p, i, c_ref: (p * steps + i, 0))
    else:
        tr = _row_tile(rows)
        grid = (1, rows // tr)
        half_spec = pl.BlockSpec((tr, cols // 2), lambda p, i, c_ref: (i, 0))
        full_spec = pl.BlockSpec((tr, cols), lambda p, i, c_ref: (i, 0))
    return pl.pallas_call(
        body, name=name,
        grid_spec=pltpu.PrefetchScalarGridSpec(
            num_scalar_prefetch=1, grid=grid,
            in_specs=[half_spec, half_spec, full_spec, full_spec, full_spec],
            out_specs=[full_spec] * 4),
        out_shape=[jax.ShapeDtypeStruct(w.shape, F32)] * 4,
        compiler_params=_cparams(dimension_semantics=("parallel", "parallel")),
    )(c_arr, own, other, w, m, v)


def _adamw_whole(items, name):
    n = len(items)

    def body(*refs):
        ins, outs = refs[:4 * n], refs[4 * n:]
        for k in range(n):
            g, w, m, v = (r[...] for r in ins[4 * k:4 * k + 4])
            d, m1, v1 = _adam_math(g, w, m, v)
            outs[3 * k][...] = d
            outs[3 * k + 1][...] = m1
            outs[3 * k + 2][...] = v1

    flat = [a for it in items for a in it]
    shapes = [jax.ShapeDtypeStruct(it[1].shape, F32) for it in items for _ in range(3)]
    out = pl.pallas_call(
        body, name=name, out_shape=shapes,
        in_specs=[VMEM_SPEC] * (4 * n), out_specs=[VMEM_SPEC] * (3 * n),
        compiler_params=_cparams(),
    )(*flat)
    return [tuple(out[3 * k:3 * k + 3]) for k in range(n)]


def _adamw_tiled(g, w, m, v, name):
    rows, cols = w.shape
    tr = _row_tile(rows)

    def body(g_ref, w_ref, m_ref, v_ref, d_out, m_out, v_out):
        d, m1, v1 = _adam_math(g_ref[...], w_ref[...], m_ref[...], v_ref[...])
        d_out[...] = d
        m_out[...] = m1
        v_out[...] = v1

    spec = pl.BlockSpec((tr, cols), lambda i: (i, 0))
    return pl.pallas_call(
        body, name=name, grid=(rows // tr,),
        out_shape=[jax.ShapeDtypeStruct(w.shape, F32)] * 3,
        in_specs=[spec] * 4, out_specs=[spec] * 3,
        compiler_params=_cparams(dimension_semantics=("parallel",)),
    )(g, w, m, v)


def _mod_forward(cond, w_mod, b_mod_cols, name):
    def body(c_ref, w_ref, b_ref, o_ref):
        o_ref[...] = _dot(_silu(c_ref[...]), w_ref[...]) + b_ref[...]

    return pl.pallas_call(
        body, name=name, out_shape=jax.ShapeDtypeStruct((cond.shape[0], w_mod.shape[1]), F32),
        in_specs=[VMEM_SPEC] * 3, out_specs=VMEM_SPEC, compiler_params=_cparams(),
    )(cond, w_mod, b_mod_cols)


def _mod_backward(cond, w_mod, dmod_cols, name):
    def body(c_ref, w_ref, d_ref, gw_ref, gc_ref):
        s = _silu(c_ref[...])
        d = d_ref[...]
        gw_ref[...] = _dot_tn(s, d)
        gc_ref[...] = _dot_nt(d[8:16, :], w_ref[...])

    return pl.pallas_call(
        body, name=name,
        out_shape=[jax.ShapeDtypeStruct(w_mod.shape, F32), jax.ShapeDtypeStruct((8, w_mod.shape[0]), F32)],
        in_specs=[VMEM_SPEC] * 3, out_specs=[VMEM_SPEC] * 2, compiler_params=_cparams(),
    )(cond, w_mod, dmod_cols)


def _col_chunks(width, step=512):
    return [(s, min(step, width - s)) for s in range(0, width, step)]


def _w_in_row(p_off):
    if p_off < 9 * HW:
        return p_off
    return 9 * HW if p_off == OFF_LR else p_off + 2 * RANK


def _in_projection(ctx0, x0, modc, modx, pre1, w_t, n_ctx_tiles, name):
    d = x0.shape[1]
    rows = ctx0.shape[0] + x0.shape[0]
    width = P_WIDTH

    def body(ctx_ref, x_ref, modc_ref, modx_ref, pre_ref, w_ref, h_ref, p_ref):
        is_ctx = pl.program_id(0) < n_ctx_tiles
        n, _ = _rms(jnp.where(is_ctx, ctx_ref[...], x_ref[...]))
        shift = jnp.where(is_ctx, modc_ref[0:1, :], modx_ref[0:1, :])
        scale = jnp.where(is_ctx, modc_ref[1:2, :], modx_ref[1:2, :])
        h = (n * pre_ref[...] * (1.0 + scale) + shift).astype(BF16)
        h_ref[...] = h
        for s, w in _col_chunks(width):
            p_ref[:, s:s + w] = _dot_nt(h, w_ref[_w_in_row(s):_w_in_row(s) + w, :])

    row = lambda i: (i, 0)
    fixed = lambda i: (0, 0)
    return pl.pallas_call(
        body, name=name, grid=(rows // TM,),
        out_shape=[jax.ShapeDtypeStruct((rows, d), BF16), jax.ShapeDtypeStruct((rows, width), F32)],
        in_specs=[pl.BlockSpec((TM, d), lambda i: (jnp.minimum(i, n_ctx_tiles - 1), 0)),
                  pl.BlockSpec((TM, d), lambda i: (jnp.maximum(i - n_ctx_tiles, 0), 0)),
                  pl.BlockSpec((8, d), fixed), pl.BlockSpec((8, d), fixed), pl.BlockSpec((1, d), fixed), VMEM_SPEC],
        out_specs=[pl.BlockSpec((TM, d), row), pl.BlockSpec((TM, width), row)],
        compiler_params=_cparams(dimension_semantics=("parallel",)),
    )(ctx0, x0, modc, modx, pre1, w_t)


C_HQ, C_HI, C_HF_FW, C_HF_BW, C_HGATE, C_GQ, C_GK, C_GV, C_GGATE = range(9)
OFF_GATE_HG = 9 * HW
OFF_LR = 13 * HW
P_WIDTH = OFF_LR + 128


def _head_norm_fwd(o, w):
    outs, ns, rs = [], [], []
    for h in range(NH):
        n, r = _rms(o[:, h * HD:(h + 1) * HD])
        ns.append(n)
        rs.append(r)
        outs.append(n * w)
    return jnp.concatenate(outs, axis=1), ns, rs


def _mixer_tail(z, o_hg, o_gla, p_hgate, p_ggate, p_gate_hg, p_gate_gla, hg_on, gla_on, wbh, wbg, wout):
    on_hg, n_hg, r_hg = _head_norm_fwd(o_hg, hg_on)
    on_gla, n_gla, r_gla = _head_norm_fwd(o_gla, gla_on)
    og_hg = (on_hg * _silu(p_hgate)).astype(BF16)
    og_gla = (on_gla * _silu(p_ggate)).astype(BF16)
    b_hg = jnp.dot(og_hg, wbh, preferred_element_type=F32)
    b_gla = jnp.dot(og_gla, wbg, preferred_element_type=F32)
    s_hg = _sigmoid(p_gate_hg)
    s_gla = _sigmoid(p_gate_gla)
    merged = (s_hg * b_hg + s_gla * b_gla).astype(BF16)
    y1 = jnp.dot(merged, wout, preferred_element_type=F32)
    return dict(on_hg=on_hg, n_hg=n_hg, r_hg=r_hg, on_gla=on_gla, n_gla=n_gla, r_gla=r_gla, og_hg=og_hg,
                og_gla=og_gla, b_hg=b_hg, b_gla=b_gla, s_hg=s_hg, s_gla=s_gla, merged=merged, y1=y1)


def _mixer_tail_fwd(x_lat, p, o_list, modx, norms, onorms, w_br_hg, w_br_gla, w_out, n_ctx_tiles, name):
    rows, d = x_lat.shape

    def body(x_ref, ofw_hg, obw_hg, ofw_gla, obw_gla, p_hgate, p_ggate, p_ghg_a, p_ghg_b, p_ggla_a, p_ggla_b,
             modx_ref, norm_ref, on_ref, wbh_ref, wbg_ref, wout_ref, z2_ref, y1_ref, mrg_ref, oghg_ref, oggla_ref):
        p_gate_hg = jnp.concatenate([p_ghg_a[...], p_ghg_b[...]], axis=1)
        p_gate_gla = jnp.concatenate([p_ggla_a[...], p_ggla_b[...]], axis=1)
        t = _mixer_tail(x_ref[...], ofw_hg[...] + obw_hg[...], ofw_gla[...] + obw_gla[...], p_hgate[...],
                        p_ggate[...], p_gate_hg, p_gate_gla, on_ref[0:1, 0:HD], on_ref[1:2, 0:HD],
                        wbh_ref[...], wbg_ref[...], wout_ref[...])
        y1_ref[...] = t["y1"]
        mrg_ref[...] = t["merged"]
        oghg_ref[...] = t["og_hg"]
        oggla_ref[...] = t["og_gla"]
        n1, _ = _rms(t["y1"])
        z2_ref[...] = x_ref[...] + n1 * norm_ref[1:2, :] * modx_ref[2:3, :]

    lat = lambda i: (i, 0)
    full = lambda i: (i + n_ctx_tiles, 0)
    fixed = lambda i: (0, 0)

    def pcol(blk):
        return pl.BlockSpec((TM, HW), lambda i: (i + n_ctx_tiles, blk))

    in_specs = ([pl.BlockSpec((TM, d), lat)] + [pl.BlockSpec((TM, HW), full)] * 4
                + [pcol(C_HGATE), pcol(C_GGATE), pcol(9), pcol(10), pcol(11), pcol(12)]
                + [pl.BlockSpec((8, d), fixed)] * 3 + [VMEM_SPEC] * 3)
    bf = lambda w: jax.ShapeDtypeStruct((rows, w), BF16)
    f32 = jax.ShapeDtypeStruct((rows, d), F32)
    return pl.pallas_call(
        body, name=name, grid=(rows // TM,), out_shape=[f32, f32, bf(d), bf(HW), bf(HW)], in_specs=in_specs,
        out_specs=[pl.BlockSpec((TM, d), lat)] * 3 + [pl.BlockSpec((TM, HW), lat)] * 2,
        compiler_params=_cparams(dimension_semantics=("parallel",)),
    )(x_lat, *o_list, p, p, p, p, p, p, modx, norms, onorms, w_br_hg, w_br_gla, w_out)


def _ffn_fwd_bwd(z2, modx, norms, w_gate, w_up, w_down, target, name):
    rows, d = z2.shape
    dff = w_gate.shape[0]
    inv_d = 1.0 / d

    def body(z2_ref, modx_ref, norm_ref, wg_ref, wu_ref, wd_ref, t_ref,
             loss_ref, dz2_ref, h2_ref, a_ref, du_ref, dv_ref, dy2_ref, stat_ref):
        i = pl.program_id(0)
        pre2, post2 = norm_ref[2:3, :], norm_ref[3:4, :]
        shift2, scale2, gate2 = modx_ref[3:4, :], modx_ref[4:5, :], modx_ref[5:6, :]
        z2 = z2_ref[...]
        n2, r2 = _rms(z2)
        nw2 = n2 * pre2
        h2 = (nw2 * (1.0 + scale2) + shift2).astype(BF16)
        h2_ref[...] = h2
        u = _dot_nt(h2, wg_ref[...])
        v = _dot_nt(h2, wu_ref[...])
        su = _silu(u)
        a = (su * v).astype(BF16)
        a_ref[...] = a
        y2 = jnp.dot(a, wd_ref[...], preferred_element_type=F32)
        n3, r3 = _rms(y2)
        err = z2 + n3 * post2 * gate2 - t_ref[...]
        part = 0.5 * inv_d * jnp.sum(err * err)
        dz3 = err * inv_d
        dgate2 = _colsum(dz3 * n3 * post2)
        tt = dz3 * gate2
        dpost2 = _colsum(tt * n3)
        dy2 = _rms_bwd(tt * post2, n3, r3).astype(BF16)
        dy2_ref[...] = dy2
        da = _dot_nt(dy2, wd_ref[...])
        du = (da * v * _dsilu(u)).astype(BF16)
        dv = (da * su).astype(BF16)
        du_ref[...] = du
        dv_ref[...] = dv
        dh2 = (jnp.dot(du, wg_ref[...], preferred_element_type=F32)
               + jnp.dot(dv, wu_ref[...], preferred_element_type=F32))
        dshift2 = _colsum(dh2)
        dscale2 = _colsum(dh2 * nw2)
        dnw2 = dh2 * (1.0 + scale2)
        dpre2 = _colsum(dnw2 * n2)
        dz2_ref[...] = dz3 + _rms_bwd(dnw2 * pre2, n2, r2)

        @pl.when(i == 0)
        def _():
            stat_ref[...] = jnp.zeros_like(stat_ref)
            loss_ref[...] = jnp.zeros_like(loss_ref)

        for r, val in enumerate((dshift2, dscale2, dgate2, dpre2, dpost2)):
            stat_ref[r:r + 1, :] += val
        loss_ref[...] += part
        stat_ref[5:6, 0:128] += part

    lat = lambda i: (i, 0)
    fixed = lambda i: (0, 0)
    bf = lambda w: jax.ShapeDtypeStruct((rows, w), BF16)
    return pl.pallas_call(
        body, name=name, grid=(rows // TM,),
        out_shape=[jax.ShapeDtypeStruct((8, 128), F32), jax.ShapeDtypeStruct((rows, d), F32), bf(d), bf(dff), bf(dff),
                   bf(dff), bf(d), jax.ShapeDtypeStruct((8, d), F32)],
        in_specs=[pl.BlockSpec((TM, d), lat), pl.BlockSpec((8, d), fixed), pl.BlockSpec((8, d), fixed)]
        + [VMEM_SPEC] * 3 + [pl.BlockSpec((TM, d), lat)],
        out_specs=[pl.BlockSpec((8, 128), fixed), pl.BlockSpec((TM, d), lat), pl.BlockSpec((TM, d), lat),
                   pl.BlockSpec((TM, dff), lat), pl.BlockSpec((TM, dff), lat), pl.BlockSpec((TM, dff), lat),
                   pl.BlockSpec((TM, d), lat), pl.BlockSpec((8, d), fixed)],
        compiler_params=_cparams(dimension_semantics=("arbitrary",)),
    )(z2, modx, norms, w_gate, w_up, w_down, target)


def _mixer_tail_bwd(x_lat, p, o_list, dz2, y1, modx, norms, onorms, w_br_hg, w_br_gla, w_out, n_ctx_tiles, n_tiles,
                    name):
    rows, d = x_lat.shape
    total = n_tiles * TM

    def body(x_ref, ofw_hg, obw_hg, ofw_gla, obw_gla, p_hgate, p_ggate, p_ghg_a, p_ghg_b, p_ggla_a, p_ggla_b,
             dz2_ref, y1_ref, modx_ref, norm_ref, on_ref, wbh_ref, wbg_ref, wout_ref,
             dohg_ref, dogla_ref, dhgate_ref, dggate_ref, dghg_ref, dggla_ref, dy1_ref, dbhg_ref, dbgla_ref,
             stat_ref):
        i = pl.program_id(0)

        @pl.when(i == 0)
        def _():
            stat_ref[...] = jnp.zeros_like(stat_ref)

        @pl.when(i < n_ctx_tiles)
        def _():
            for ref in (dohg_ref, dogla_ref, dhgate_ref, dggate_ref, dghg_ref, dggla_ref):
                ref[...] = jnp.zeros_like(ref)

        @pl.when(i >= n_ctx_tiles)
        def _():
            post1, gate1 = norm_ref[1:2, :], modx_ref[2:3, :]
            hg_on, gla_on = on_ref[0:1, 0:HD], on_ref[1:2, 0:HD]
            p_gate_hg = jnp.concatenate([p_ghg_a[...], p_ghg_b[...]], axis=1)
            p_gate_gla = jnp.concatenate([p_ggla_a[...], p_ggla_b[...]], axis=1)
            ph, pg = p_hgate[...], p_ggate[...]
            t = _mixer_tail(x_ref[...], ofw_hg[...] + obw_hg[...], ofw_gla[...] + obw_gla[...], ph, pg,
                            p_gate_hg, p_gate_gla, hg_on, gla_on, wbh_ref[...], wbg_ref[...], wout_ref[...])
            dz2 = dz2_ref[...]
            n1, r1 = _rms(y1_ref[...])
            dgate1 = _colsum(dz2 * n1 * post1)
            tt = dz2 * gate1
            dpost1 = _colsum(tt * n1)
            dy1 = _rms_bwd(tt * post1, n1, r1).astype(BF16)
            dy1_ref[...] = dy1
            dmerged = _dot_nt(dy1, wout_ref[...])
            dghg_ref[...] = (dmerged * t["b_hg"] * t["s_hg"] * (1.0 - t["s_hg"])).astype(BF16)
            dggla_ref[...] = (dmerged * t["b_gla"] * t["s_gla"] * (1.0 - t["s_gla"])).astype(BF16)
            db_hg = (dmerged * t["s_hg"]).astype(BF16)
            db_gla = (dmerged * t["s_gla"]).astype(BF16)
            dbhg_ref[...] = db_hg
            dbgla_ref[...] = db_gla
            don_acc = []
            for (db, wb, pgate, on, ns, rs, gain, gate_ref, do_ref) in (
                    (db_hg, wbh_ref, ph, t["on_hg"], t["n_hg"], t["r_hg"], hg_on, dhgate_ref, dohg_ref),
                    (db_gla, wbg_ref, pg, t["on_gla"], t["n_gla"], t["r_gla"], gla_on, dggate_ref, dogla_ref)):
                dog = _dot_nt(db, wb[...])
                gate_ref[...] = (dog * on * _dsilu(pgate)).astype(BF16)
                don = dog * _silu(pgate)
                acc = jnp.zeros((1, HD), F32)
                for h in range(NH):
                    sl = slice(h * HD, (h + 1) * HD)
                    acc = acc + _colsum(don[:, sl] * ns[h])
                    do_ref[:, sl] = _rms_bwd(don[:, sl] * gain, ns[h], rs[h]).astype(BF16)
                don_acc.append(acc)
            stat_ref[0:1, :] += dgate1
            stat_ref[1:2, :] += dpost1
            stat_ref[2:3, 0:HD] += don_acc[0]
            stat_ref[2:3, HD:2 * HD] += don_acc[1]

    lat = lambda i: (jnp.maximum(i - n_ctx_tiles, 0), 0)
    full = lambda i: (i, 0)
    fixed = lambda i: (0, 0)

    def pcol(blk):
        return pl.BlockSpec((TM, HW), lambda i: (i, blk))

    in_specs = ([pl.BlockSpec((TM, d), lat)] + [pl.BlockSpec((TM, HW), full)] * 4
                + [pcol(C_HGATE), pcol(C_GGATE), pcol(9), pcol(10), pcol(11), pcol(12)]
                + [pl.BlockSpec((TM, d), lat), pl.BlockSpec((TM, d), lat)]
                + [pl.BlockSpec((8, d), fixed)] * 3 + [VMEM_SPEC] * 3)
    f = lambda w: jax.ShapeDtypeStruct((total, w), BF16)
    out_shape = [f(HW), f(HW), f(HW), f(HW), f(d), f(d), jax.ShapeDtypeStruct((rows, d), BF16),
                 jax.ShapeDtypeStruct((rows, d), BF16), jax.ShapeDtypeStruct((rows, d), BF16),
                 jax.ShapeDtypeStruct((8, d), F32)]
    out_specs = ([pl.BlockSpec((TM, HW), full)] * 4 + [pl.BlockSpec((TM, d), full)] * 2
                 + [pl.BlockSpec((TM, d), lat)] * 3 + [pl.BlockSpec((8, d), fixed)])
    return pl.pallas_call(
        body, name=name, grid=(n_tiles,), out_shape=out_shape, in_specs=in_specs, out_specs=out_specs,
        compiler_params=_cparams(dimension_semantics=("arbitrary",)),
    )(x_lat, *o_list, p, p, p, p, p, p, dz2, y1, modx, norms, onorms, w_br_hg, w_br_gla, w_out)


def _in_projection_bwd(ctx0, x0, dz2, modc, modx, pre1, w_t, pieces, n_ctx_tiles, name):
    d = x0.shape[1]
    rows = ctx0.shape[0] + x0.shape[0]
    lat_rows = dz2.shape[0]
    width = P_WIDTH
    n_pieces = len(pieces)

    def body(*refs):
        ctx_ref, x_ref, dz2_ref, modc_ref, modx_ref, pre_ref, w_ref = refs[:7]
        (dhq_f, dhq_b, dhi_f, dhi_b, dhf_f, dhf_b, dhgate, dgq_f, dgq_b, dgk_f, dgk_b, dgv_f, dgv_b, dggate,
         dghg, dggla, dlr_f, dlr_b) = refs[7:7 + n_pieces]
        dp_ref, gx_ref, stat_ref = refs[7 + n_pieces:]
        i = pl.program_id(0)
        is_ctx = i < n_ctx_tiles
        z = jnp.where(is_ctx, ctx_ref[...], x_ref[...])
        sections = [
            (0, dhq_f[...] + dhq_b[...]), (HW, dhi_f[...] + dhi_b[...]), (2 * HW, dhf_f[...]), (3 * HW, dhf_b[...]),
            (4 * HW, dhgate[...]), (5 * HW, dgq_f[...] + dgq_b[...]), (6 * HW, dgk_f[...] + dgk_b[...]),
            (7 * HW, dgv_f[...] + dgv_b[...]), (8 * HW, dggate[...]),
            (9 * HW, dghg[:, 0:HW]), (10 * HW, dghg[:, HW:2 * HW]),
            (11 * HW, dggla[:, 0:HW]), (12 * HW, dggla[:, HW:2 * HW]), (OFF_LR, dlr_f[...] + dlr_b[...])]
        dh = jnp.zeros((TM, d), F32)
        for off, val in sections:
            w = val.shape[1]
            vb = val.astype(BF16)
            dp_ref[off:off + w, :] = vb.T
            dh = dh + jnp.dot(vb, w_ref[_w_in_row(off):_w_in_row(off) + w, :], preferred_element_type=F32)
        n, r = _rms(z)
        pre = pre_ref[...]
        scale = jnp.where(is_ctx, modc_ref[1:2, :], modx_ref[1:2, :])
        nw = n * pre
        dshift = _colsum(dh)
        dscale = _colsum(dh * nw)
        dnw = dh * (1.0 + scale)
        dpre = _colsum(dnw * n)
        gx_ref[...] = dz2_ref[...] + _rms_bwd(dnw * pre, n, r)
        zero = jnp.zeros((1, d), F32)

        @pl.when(i == 0)
        def _():
            stat_ref[...] = jnp.zeros_like(stat_ref)

        stat_ref[0:1, :] += jnp.where(is_ctx, zero, dshift)
        stat_ref[1:2, :] += jnp.where(is_ctx, zero, dscale)
        stat_ref[2:3, :] += jnp.where(is_ctx, dshift, zero)
        stat_ref[3:4, :] += jnp.where(is_ctx, dscale, zero)
        stat_ref[4:5, :] += dpre

    full = lambda i: (i, 0)
    lat = lambda i: (jnp.maximum(i - n_ctx_tiles, 0), 0)
    fixed = lambda i: (0, 0)
    piece_specs = [pl.BlockSpec((TM, a.shape[1]), full) for a in pieces]
    in_specs = [pl.BlockSpec((TM, d), lambda i: (jnp.minimum(i, n_ctx_tiles - 1), 0)), pl.BlockSpec((TM, d), lat),
                pl.BlockSpec((TM, d), lat), pl.BlockSpec((8, d), fixed),
                pl.BlockSpec((8, d), fixed), pl.BlockSpec((1, d), fixed), VMEM_SPEC] + piece_specs
    return pl.pallas_call(
        body, name=name, grid=(rows // TM,),
        out_shape=[jax.ShapeDtypeStruct((width, rows), BF16), jax.ShapeDtypeStruct((lat_rows, d), F32),
                   jax.ShapeDtypeStruct((8, d), F32)],
        in_specs=in_specs,
        out_specs=[pl.BlockSpec((width, TM), lambda i: (0, i)), pl.BlockSpec((TM, d), lat),
                   pl.BlockSpec((8, d), fixed)],
        compiler_params=_cparams(dimension_semantics=("arbitrary",)),
    )(ctx0, x0, dz2, modc, modx, pre1, w_t, *pieces)


def _transposed_lhs_matmul(x_ref, dy_ref, o_ref, xt_ref):
    @pl.when(pl.program_id(1) == 0)
    def _():
        xt_ref[...] = x_ref[...].T

    o_ref[...] = jnp.dot(xt_ref[...], dy_ref[...], preferred_element_type=F32)


def _w_in_grad(dp_t, h1, n_cols, name):
    rows, d = h1.shape
    n_main = OFF_LR // HW
    lr0 = _w_in_row(OFF_LR)

    def body(x_ref, xlr_ref, h_ref, o_hbm, acc_ref, sem):
        i = pl.program_id(0)

        def main_copy(step):
            row = jnp.where(step < 9, step * HW, step * HW + 2 * RANK)
            return pltpu.make_async_copy(acc_ref, o_hbm.at[pl.ds(pl.multiple_of(row, 8), HW), :], sem)

        lr_copy = pltpu.make_async_copy(acc_ref.at[0:2 * RANK, :], o_hbm.at[lr0:lr0 + 2 * RANK, :], sem)

        @pl.when(i > 0)
        def _():
            main_copy(i - 1).wait()

        @pl.when(i < n_main)
        def _():
            acc_ref[...] = jnp.dot(x_ref[...], h_ref[...], preferred_element_type=F32)
            main_copy(i).start()

        @pl.when(i == n_main)
        def _():
            acc_ref[0:128, :] = jnp.dot(xlr_ref[...], h_ref[...], preferred_element_type=F32)
            lr_copy.start()
            lr_copy.wait()

    return pl.pallas_call(
        body, name=name, grid=(n_main + 1,),
        out_shape=jax.ShapeDtypeStruct((n_cols, d), F32),
        in_specs=[pl.BlockSpec((HW, rows), lambda i: (jnp.minimum(i, n_main - 1), 0)),
                  pl.BlockSpec((128, rows), lambda i: (OFF_LR // 128, 0)),
                  pl.BlockSpec((rows, d), lambda i: (0, 0))],
        out_specs=ANY_SPEC,
        scratch_shapes=[pltpu.VMEM((HW, d), F32), pltpu.SemaphoreType.DMA],
        compiler_params=_cparams(dimension_semantics=("arbitrary",)),
    )(dp_t, dp_t, h1)


def _weight_grad(xs, dy, name, tk=None, tn=512, k_first=0, k_tiles=None):
    rows = dy.shape[0]
    n = dy.shape[1]
    tn_ = min(tn, n)
    tk_ = xs.shape[1] if tk is None else tk
    k_tiles = xs.shape[1] // tk_ if k_tiles is None else k_tiles
    k = k_tiles * tk_

    return pl.pallas_call(
        functools.partial(_transposed_lhs_matmul), name=name, grid=(k_tiles, n // tn_),
        out_shape=jax.ShapeDtypeStruct((k, n), F32),
        in_specs=[pl.BlockSpec((rows, tk_), lambda i, j: (0, i + k_first)),
                  pl.BlockSpec((rows, tn_), lambda i, j: (0, j))],
        out_specs=pl.BlockSpec((tk_, tn_), lambda i, j: (i, j)),
        scratch_shapes=[pltpu.VMEM((tk_, rows), BF16)],
        compiler_params=_cparams(dimension_semantics=("parallel", "arbitrary")),
    )(xs, dy)


def _running_sums(xs, fws):
    c = xs[0].shape[0]
    row = lax.broadcasted_iota(jnp.int32, (c, 1), 0)
    s = 1
    while s < c:
        xs = [x + (jnp.where(row >= s, pltpu.roll(x, s, axis=0), 0.0) if fw else
                   jnp.where(row < c - s, pltpu.roll(x, c - s, axis=0), 0.0)) for x, fw in zip(xs, fws)]
        s *= 2
    return xs


def _chunks_terms(qs, ks, gs, fws):
    c = CHUNK
    n = len(qs)
    r = lax.broadcasted_iota(jnp.int32, (c, c), 0)
    s = lax.broadcasted_iota(jnp.int32, (c, c), 1)
    row = lax.broadcasted_iota(jnp.int32, (c, 1), 0)
    per_dir = {}
    for fw in set(fws):
        pos = row if fw else (c - 1 - row)
        per_dir[fw] = dict(
            causal=(s <= r) if fw else (s >= r), causal_t=(s >= r) if fw else (s <= r), pos=pos,
            in_blk=[(pos >= SUB * j) & (pos < SUB * (j + 1)) for j in range(NSUB)],
            start_row=[None] + [SUB * j - 1 if fw else c - SUB * j for j in range(1, NSUB)],
            rend=c - 1 if fw else 0)
    dirs = [per_dir[fw] for fw in fws]
    cums = _running_sums(gs, fws)
    starts = [[None] + [cum[d["start_row"][j]:d["start_row"][j] + 1, :] for j in range(1, NSUB)]
              for cum, d in zip(cums, dirs)]
    es = [[jnp.exp(cum) for cum in cums]]
    for j in range(1, NSUB):
        es.append([jnp.exp(jnp.where(d["pos"] >= SUB * j, cum - st[j], -1e30)) for cum, st, d in zip(cums, starts, dirs)])
    owns = [functools.reduce(lambda rest, j: jnp.where(d["in_blk"][j], st[j], rest), range(1, NSUB), 0.0)
            for st, d in zip(starts, dirs)]
    kscales = [jnp.exp(own - cum) for own, cum in zip(owns, cums)]
    cends = [cum[d["rend"]:d["rend"] + 1, :] for cum, d in zip(cums, dirs)]
    tails = [jnp.exp(cend - cum) for cend, cum in zip(cends, cums)]
    qcats = [jnp.concatenate([q * es[j][i] for j in range(NSUB)], axis=1).astype(BF16) for i, q in enumerate(qs)]
    kts = [k * ksc for k, ksc in zip(ks, kscales)]
    kms = [jnp.concatenate([jnp.where(d["in_blk"][j], kt, 0.0) for j in range(NSUB)], axis=1).astype(BF16)
           for kt, d in zip(kts, dirs)]
    e_by_lane = [[es[j][i] for j in range(NSUB)] for i in range(n)]
    return dict(dirs=dirs, e=e_by_lane, kscale=kscales, cend=cends, tail=tails, qcat=qcats, km=kms, kt=kts)


def _chunks_fwd(qs, ks, vs, gs, st0s, fws):
    t = _chunks_terms(qs, ks, gs, fws)
    scores = [_dot_nt(qc, km) for qc, km in zip(t["qcat"], t["km"])]
    a = [jnp.where(d["causal"], sc, 0.0) for sc, d in zip(scores, t["dirs"])]
    inter = [_dot_nt(qc[:, 0:HD], st0) for qc, st0 in zip(t["qcat"], st0s)]
    intra = [_dot(a_, v) for a_, v in zip(a, vs)]
    os_ = [x + y for x, y in zip(intra, inter)]
    upd = [_dot_tn(v, k * tl) for v, k, tl in zip(vs, ks, t["tail"])]
    st1s = [st0 * jnp.exp(ce) + u for st0, ce, u in zip(st0s, t["cend"], upd)]
    return os_, st1s


def _chunks_bwd(qs, ks, vs, gs, st0s, dos, dst1s, fws):
    n = len(qs)
    t = _chunks_terms(qs, ks, gs, fws)
    qcat, km, e, dirs = t["qcat"], t["km"], t["e"], t["dirs"]
    a_t = [jnp.where(d["causal_t"], _dot_nt(km_, qc), 0.0) for km_, qc, d in zip(km, qcat, dirs)]
    ktail = [k * tl for k, tl in zip(ks, t["tail"])]
    dv_a = [_dot(at, do) for at, do in zip(a_t, dos)]
    dv_b = [_dot_nt(kt, ds) for kt, ds in zip(ktail, dst1s)]
    dv = [x + y for x, y in zip(dv_a, dv_b)]
    da = [jnp.where(d["causal"], _dot_nt(do, v), 0.0) for do, v, d in zip(dos, vs, dirs)]
    da_t = [jnp.where(d["causal_t"], _dot_nt(v, do), 0.0) for do, v, d in zip(dos, vs, dirs)]
    dqcat = [_dot(da_, km_) for da_, km_ in zip(da, km)]
    dq_inter = [e[i][0] * _dot(dos[i], st0s[i]) for i in range(n)]
    dkm = [_dot(dat, qc) for dat, qc in zip(da_t, qcat)]
    dk_inter = [_dot(v, ds) * tl for v, ds, tl in zip(vs, dst1s, t["tail"])]
    dq = [dq_inter[i] + sum(e[i][j] * dqcat[i][:, j * HD:(j + 1) * HD] for j in range(NSUB)) for i in range(n)]
    dkt = [sum(jnp.where(dirs[i]["in_blk"][j], dkm[i][:, j * HD:(j + 1) * HD], 0.0) for j in range(NSUB))
           for i in range(n)]
    dk = [dkt[i] * t["kscale"][i] + dk_inter[i] for i in range(n)]
    dcum = [qs[i] * dq_inter[i] - ks[i] * dk_inter[i] - t["kt"][i].astype(BF16).astype(F32) * dkt[i]
            + sum(qcat[i][:, j * HD:(j + 1) * HD].astype(F32) * dqcat[i][:, j * HD:(j + 1) * HD] for j in range(NSUB))
            for i in range(n)]
    ecend = [jnp.exp(ce) for ce in t["cend"]]
    end = [ecend[i] * _colsum(st0s[i] * dst1s[i]) + _colsum(ks[i] * dk_inter[i]) for i in range(n)]
    sums = _running_sums(dcum, [not fw for fw in fws])
    dg = [sm + en for sm, en in zip(sums, end)]
    upd = [_dot_tn(dos[i], qs[i] * e[i][0]) for i in range(n)]
    dst0 = [dst1s[i] * ecend[i] + upd[i] for i in range(n)]
    return dq, dk, dv, dg, dst0


def _chunk_index(step, n_ctx_chunks, n_chunks, fw):
    if fw:
        return step
    return jnp.where(step < n_ctx_chunks, n_ctx_chunks - 1 - step, n_chunks - 1 + n_ctx_chunks - step)


def _hg_inputs(hq, hf, lbv, d_idx, sl):
    lb = _sigmoid(lbv[d_idx:d_idx + 1, sl] - lbv[2 + d_idx:3 + d_idx, sl])
    sg = _sigmoid(hf)
    f = lb + (1.0 - lb) * sg
    return _silu(hq), 1.0 - f, jnp.log(f), f, sg, lb


def _scan_fwd_both(p, branch_sides, n_ctx_chunks, name):
    rows = p.shape[0]
    n_chunks = rows // CHUNK
    n_ins = [4 if branch == "hg" else 6 for branch, _ in branch_sides]
    n_in_all = 2 * sum(n_ins)
    n_br = len(branch_sides)

    def body(*refs):
        ins, outs, state = refs[:n_in_all], refs[n_in_all:n_in_all + 4 * n_br], refs[-1]

        @pl.when(pl.program_id(0) == 0)
        def _():
            state[...] = jnp.zeros_like(state)

        lanes, where = [], []
        pos = 0
        for bi, (branch, _) in enumerate(branch_sides):
            hg = branch == "hg"
            n_in = n_ins[bi]
            for di, fw in enumerate((True, False)):
                r = ins[pos:pos + n_in]
                pos += n_in
                o_ref, st_ref = outs[4 * bi + 2 * di], outs[4 * bi + 2 * di + 1]
                if hg:
                    a_ref, b_ref, c_ref, lb_ref = r
                else:
                    a_ref, b_ref, c_ref, lr_ref, wgk_ref, bgk_ref = r
                    logits = _dot(lr_ref[...], wgk_ref[...]) + bgk_ref[...]
                    g_all = _log_sigmoid(logits) * (1.0 / GATE_NORM)
                for h in range(NH):
                    sl = slice(h * HD, (h + 1) * HD)
                    if hg:
                        q, k, g, _, _, _ = _hg_inputs(a_ref[:, sl], c_ref[:, sl], lb_ref[...], di, sl)
                        v = b_ref[:, sl]
                    else:
                        q, k, v, g = a_ref[:, sl] * (HD ** -0.5), b_ref[:, sl], c_ref[:, sl], g_all[:, sl]
                    lanes.append((q, k, v, g, state[2 * bi + di, h], fw))
                    where.append((2 * bi + di, h, sl, o_ref, st_ref))
        qs, ks, vs, gs, st0s, fws = (list(col) for col in zip(*lanes))
        os_, st1s = _chunks_fwd(qs, ks, vs, gs, st0s, fws)
        for (si, h, sl, o_ref, st_ref), st0, o, st1 in zip(where, st0s, os_, st1s):
            st_ref[0, h] = st0
            o_ref[:, sl] = o
            state[si, h] = st1

    fixed = lambda j: (0, 0)
    in_specs, args, out_specs = [], [], []
    for branch, side in branch_sides:
        for di, fw in enumerate((True, False)):
            chunk = functools.partial(_chunk_index, n_ctx_chunks=n_ctx_chunks, n_chunks=n_chunks, fw=fw)

            def cmap(blk, width=HW, chunk=chunk):
                return pl.BlockSpec((CHUNK, width), lambda j: (chunk(j), blk))

            if branch == "hg":
                in_specs += [cmap(C_HQ), cmap(C_HI), cmap(C_HF_FW + di), pl.BlockSpec((4, HW), fixed)]
                args += [p, p, p, side]
            else:
                in_specs += [cmap(C_GQ), cmap(C_GK), cmap(C_GV), cmap(OFF_LR // 128, 128),
                             pl.BlockSpec((128, HW), fixed), pl.BlockSpec((1, HW), fixed)]
                args += [p, p, p, p, side[di][0], side[di][1]]
            out_specs += [cmap(0), pl.BlockSpec((1, NH, HD, HD), lambda j, chunk=chunk: (chunk(j), 0, 0, 0))]
    return pl.pallas_call(
        body, name=name, grid=(n_chunks,),
        out_shape=[jax.ShapeDtypeStruct((rows, HW), F32),
                   jax.ShapeDtypeStruct((n_chunks, NH, HD, HD), F32)] * (2 * n_br),
        in_specs=in_specs, out_specs=out_specs,
        scratch_shapes=[pltpu.VMEM((2 * n_br, NH, HD, HD), F32)],
        compiler_params=_cparams(dimension_semantics=("arbitrary",)),
    )(*args)


def _scan_bwd_both(p, side, states, d_o, n_ctx_chunks, branch, name):
    rows = p.shape[0]
    n_chunks = rows // CHUNK
    hg = branch == "hg"
    n_in = 6 if hg else 8
    n_out = 4 if hg else 6

    def body(*refs):
        ins, outs, dstate = refs[:2 * n_in], refs[2 * n_in:2 * n_in + 2 * n_out], refs[-1]
        first = pl.program_id(0) == 0

        @pl.when(first)
        def _():
            dstate[...] = jnp.zeros_like(dstate)

        lanes, where, extra, ctx = [], [], [], []
        for di, fw in enumerate((True, False)):
            r, w = ins[di * n_in:(di + 1) * n_in], outs[di * n_out:(di + 1) * n_out]
            if hg:
                a_ref, b_ref, c_ref, lb_ref, st_ref, do_ref = r
                da_ref, db_ref, dc_ref, dlb_ref = w
                acc_refs = (dlb_ref,)
            else:
                a_ref, b_ref, c_ref, lr_ref, wgk_ref, bgk_ref, st_ref, do_ref = r
                da_ref, db_ref, dc_ref, dlr_ref, dwgk_ref, dbias_ref = w
                acc_refs = (dwgk_ref, dbias_ref)
                lr = lr_ref[...]
                logits = _dot(lr, wgk_ref[...]) + bgk_ref[...]
                g_all = _log_sigmoid(logits) * (1.0 / GATE_NORM)

            @pl.when(first)
            def _(acc_refs=acc_refs):
                for ref in acc_refs:
                    ref[...] = jnp.zeros_like(ref)

            for h in range(NH):
                sl = slice(h * HD, (h + 1) * HD)
                if hg:
                    hq, hf = a_ref[:, sl], c_ref[:, sl]
                    q, k, g, f, sg, lb = _hg_inputs(hq, hf, lb_ref[...], di, sl)
                    v = b_ref[:, sl]
                    extra.append((hq, f, sg, lb))
                else:
                    q, k, v, g = a_ref[:, sl] * (HD ** -0.5), b_ref[:, sl], c_ref[:, sl], g_all[:, sl]
                    extra.append(None)
                lanes.append((q, k, v, g, st_ref[0, h], do_ref[:, sl], dstate[di, h], fw))
                where.append((di, h, sl))
            ctx.append((w, None if hg else (lr, logits, wgk_ref)))

        dqs, dks, dvs, dgs, dst0s = [], [], [], [], []
        for lo in range(0, len(lanes), BWD_LANES):
            cols = [list(col) for col in zip(*lanes[lo:lo + BWD_LANES])]
            for acc, part in zip((dqs, dks, dvs, dgs, dst0s), _chunks_bwd(*cols)):
                acc.extend(part)
        dg_parts = {0: [], 1: []}
        for (di, h, sl), ex, dq, dk, dv, dg, dst0 in zip(where, extra, dqs, dks, dvs, dgs, dst0s):
            dstate[di, h] = dst0
            w = ctx[di][0]
            if hg:
                hq, f, sg, lb = ex
                da_ref, db_ref, dc_ref, dlb_ref = w
                da_ref[:, sl] = (dq * _dsilu(hq)).astype(BF16)
                db_ref[:, sl] = dv.astype(BF16)
                df = dg / f - dk
                dc_ref[:, sl] = (df * (1.0 - lb) * sg * (1.0 - sg)).astype(BF16)
                dlb_ref[0:1, sl] += _colsum(df * (1.0 - sg))
            else:
                da_ref, db_ref, dc_ref = w[:3]
                da_ref[:, sl] = (dq * (HD ** -0.5)).astype(BF16)
                db_ref[:, sl] = dk.astype(BF16)
                dc_ref[:, sl] = dv.astype(BF16)
                dg_parts[di].append(dg)
        if not hg:
            for di in range(2):
                dlr_ref, dwgk_ref, dbias_ref = ctx[di][0][3:]
                lr, logits, wgk_ref = ctx[di][1]
                dlogits = jnp.concatenate(dg_parts[di], axis=1) * (1.0 / GATE_NORM) * (1.0 - _sigmoid(logits))
                dlr_ref[...] = _dot_nt(dlogits, wgk_ref[...]).astype(BF16)
                dwgk_ref[...] += _dot_tn(lr, dlogits)
                dbias_ref[0:1, :] += _colsum(dlogits)

    fixed = lambda j: (0, 0)
    big = jax.ShapeDtypeStruct((rows, HW), BF16)
    in_specs, args, out_shape, out_specs = [], [], [], []
    for di, fw in enumerate((True, False)):
        def chunk_of(j, fw=fw):
            return _chunk_index(n_chunks - 1 - j, n_ctx_chunks, n_chunks, fw)

        def cmap(blk, width=HW, chunk_of=chunk_of):
            return pl.BlockSpec((CHUNK, width), lambda j: (chunk_of(j), blk))

        st_spec = pl.BlockSpec((1, NH, HD, HD), lambda j, chunk_of=chunk_of: (chunk_of(j), 0, 0, 0))
        if hg:
            in_specs += [cmap(C_HQ), cmap(C_HI), cmap(C_HF_FW + di), pl.BlockSpec((4, HW), fixed), st_spec, cmap(0)]
            args += [p, p, p, side, states[di], d_o]
            out_shape += [big, big, big, jax.ShapeDtypeStruct((8, HW), F32)]
            out_specs += [cmap(0), cmap(0), cmap(0), pl.BlockSpec((8, HW), fixed)]
        else:
            in_specs += [cmap(C_GQ), cmap(C_GK), cmap(C_GV), cmap(OFF_LR // 128, 128),
                         pl.BlockSpec((128, HW), fixed), pl.BlockSpec((1, HW), fixed), st_spec, cmap(0)]
            args += [p, p, p, p, side[di][0], side[di][1], states[di], d_o]
            out_shape += [big, big, big, jax.ShapeDtypeStruct((rows, 128), BF16),
                          jax.ShapeDtypeStruct((128, HW), F32), jax.ShapeDtypeStruct((8, HW), F32)]
            out_specs += [cmap(0), cmap(0), cmap(0), cmap(0, 128), pl.BlockSpec((128, HW), fixed),
                          pl.BlockSpec((8, HW), fixed)]
    return pl.pallas_call(
        body, name=name, grid=(n_chunks,), out_shape=out_shape, in_specs=in_specs, out_specs=out_specs,
        scratch_shapes=[pltpu.VMEM((2, NH, HD, HD), F32)],
        compiler_params=_cparams(dimension_semantics=("arbitrary",)),
    )(*args)


SMALL_ROWS = 56
ROWS_MOD_X = (0, 1, 8, 16, 17, 18)
ROWS_MOD_C = (2, 3)
ROW_PRE1, ROW_POST1, ROW_ONORM, ROW_PRE2, ROW_POST2, ROW_LB, ROW_BGK, ROW_WGK = 4, 9, 10, 19, 20, 24, 32, 40
ROW_LOSS = 21


def _reduce_small(gathered, lb_full, name):
    _, _, d = gathered.shape

    def body(g_ref, lb_ref, sum_ref, dmod_ref, dbmod_ref, dlb_ref):
        total = g_ref[0]
        for b in range(1, N_DEV):
            total = total + g_ref[b]
        sum_ref[...] = total
        dmod_ref[...] = jnp.zeros_like(dmod_ref)
        for m in range(N_MOD):
            col = slice(m * d, (m + 1) * d)
            acc = jnp.zeros((1, d), F32)
            for b in range(N_DEV):
                row = g_ref[b, ROWS_MOD_X[m]:ROWS_MOD_X[m] + 1, :]
                dmod_ref[b:b + 1, col] = row
                acc = acc + row
            if m < 2:
                ctx_row = total[ROWS_MOD_C[m]:ROWS_MOD_C[m] + 1, :]
                dmod_ref[8:9, col] = ctx_row
                acc = acc + ctx_row
            dbmod_ref[:, col] = acc
        lbv = lb_ref[...]
        for dd in range(2):
            lb = _sigmoid(lbv[dd:dd + 1, :] - lbv[2 + dd:3 + dd, :])
            gl = total[ROW_LB:ROW_LB + 1, dd * HW:(dd + 1) * HW] * lb * (1.0 - lb)
            dlb_ref[dd:dd + 1, :] = gl
            dlb_ref[2 + dd:3 + dd, :] = -gl

    return pl.pallas_call(
        body, name=name,
        out_shape=[jax.ShapeDtypeStruct((SMALL_ROWS, d), F32), jax.ShapeDtypeStruct((16, N_MOD * d), F32),
                   jax.ShapeDtypeStruct((1, N_MOD * d), F32), jax.ShapeDtypeStruct((4, HW), F32)],
        in_specs=[VMEM_SPEC] * 2, out_specs=[VMEM_SPEC] * 4, compiler_params=_cparams(),
    )(gathered, lb_full)


def _c_ctx_grad(gathered, c_ctx_row, name):
    def body(g_ref, c_ref, o_ref):
        acc = g_ref[0, 0:1, :]
        for chip in range(1, N_CHIP):
            acc = acc + g_ref[2 * chip, 0:1, :]
        o_ref[...] = acc * _dsilu(c_ref[...])

    return pl.pallas_call(
        body, name=name, out_shape=jax.ShapeDtypeStruct(c_ctx_row.shape, F32),
        in_specs=[VMEM_SPEC] * 2, out_specs=VMEM_SPEC, compiler_params=_cparams(),
    )(gathered, c_ctx_row)


def _blocked(full, n_blocks):
    k, n = full.shape
    return full.reshape(k, n_blocks, n // n_blocks).transpose(1, 0, 2)


def _unblocked(blocks):
    nb, k, n = blocks.shape
    return blocks.transpose(1, 0, 2).reshape(k, nb * n)


def _sample_front(x0, ctx0, modc, modx, norm_pre1, lb_full, gla_side, w_in_r):
    ctx_len = ctx0.shape[0]
    n_ctx_tiles = ctx_len // TM
    n_ctx_chunks = ctx_len // CHUNK
    h1, p = _in_projection(ctx0, x0, modc, modx, norm_pre1, w_in_r, n_ctx_tiles, "in_projection")
    (o_hg_fw, st_hg_fw, o_hg_bw, st_hg_bw, o_gla_fw, st_gla_fw, o_gla_bw, st_gla_bw) = _scan_fwd_both(
        p, [("hg", lb_full), ("gla", gla_side)], n_ctx_chunks, "scan_fwd")
    return dict(h1=h1, p=p, o_list=[o_hg_fw, o_hg_bw, o_gla_fw, o_gla_bw],
                states=[st_hg_fw, st_hg_bw, st_gla_fw, st_gla_bw])


def _sample_back(reduce, front, x0, ctx0, target0, modc, modx, norm_pre1, norms, onorms, lb_full, gla_side, w_in_r,
                 wbh, wbg, wout, ffn_weights):
    seq, d = x0.shape
    ctx_len = ctx0.shape[0]
    n_ctx_tiles = ctx_len // TM
    n_tiles = (ctx_len + seq) // TM
    n_ctx_chunks = ctx_len // CHUNK
    h1, p, o_list = front["h1"], front["p"], front["o_list"]
    st_hg_fw, st_hg_bw, st_gla_fw, st_gla_bw = front["states"]
    z2, y1, merged, og_hg, og_gla = _mixer_tail_fwd(x0, p, o_list, modx, norms, onorms, wbh, wbg, wout, n_ctx_tiles,
                                                    "mixer_tail")
    wg, wu, wd = ffn_weights([z2])
    loss_part, dz2, h2, a_act, du, dv, dy2, stat_ffn = _ffn_fwd_bwd(z2, modx, norms, wg, wu, wd, target0, "ffn")
    dff = wg.shape[0]
    tok = reduce("ffn", [_weight_grad(du, h2, "grad_w_ff_gate", tk=dff // 2, tn=d),
                         _weight_grad(dv, h2, "grad_w_ff_up", tk=dff // 2, tn=d),
                         _weight_grad(a_act, dy2, "grad_w_ff_down", tk=dff // 2)])

    (d_ohg, d_ogla, d_hgate, d_ggate, d_ghg, d_ggla, dy1, db_hg, db_gla, stat_mix) = _mixer_tail_bwd(
        x0, p, o_list, dz2, y1, modx + tok, norms, onorms, wbh, wbg, wout, n_ctx_tiles, n_tiles, "mixer_tail_bwd")
    tok = reduce("mix", [_weight_grad(og_hg, db_hg, "grad_w_br_hg"), _weight_grad(og_gla, db_gla, "grad_w_br_gla"),
                         _weight_grad(merged, dy1, "grad_w_out")])
    tok = tok + reduce("push_ffn", [dy1])
    gla_b = [(wgk, bias + tok) for wgk, bias in gla_side]
    (dgq_f, dgk_f, dgv_f, dlr_f, dwgk_f, dbgk_f, dgq_b, dgk_b, dgv_b, dlr_b, dwgk_b, dbgk_b) = _scan_bwd_both(
        p, gla_b, (st_gla_fw, st_gla_bw), d_ogla, n_ctx_chunks, "gla", "scan_gla_bwd")
    lb_b = lb_full + reduce("push_mix", [dbgk_f])
    (dhq_f, dhi_f, dhf_f, dlb_f, dhq_b, dhi_b, dhf_b, dlb_b) = _scan_bwd_both(
        p, lb_b, (st_hg_fw, st_hg_bw), d_ohg, n_ctx_chunks, "hg", "scan_hg_bwd")
    pieces = [dhq_f, dhq_b, dhi_f, dhi_b, dhf_f, dhf_b, d_hgate, dgq_f, dgq_b, dgk_f, dgk_b, dgv_f, dgv_b, d_ggate,
              d_ghg, d_ggla, dlr_f, dlr_b]
    dp, grad_x, stat_in = _in_projection_bwd(ctx0, x0, dz2, modc, modx, norm_pre1, w_in_r, pieces, n_ctx_tiles,
                                             "in_projection_bwd")

    tok = reduce("in", [_w_in_grad(dp, h1, w_in_r.shape[0], "grad_w_in")])
    reduce("small", dict(stat_in=stat_in + tok, stat_mix=stat_mix, stat_ffn=stat_ffn, dlb=(dlb_f, dlb_b),
                         dwgk=(dwgk_f, dwgk_b), dbgk=(dbgk_f, dbgk_b)))
    reduce("push_in", [])
    return dict(loss_part=loss_part, grad_x=grad_x)


def kernel(x, c, ctx, c_ctx, w_mod, b_mod, norm_pre1, norm_post1, norm_pre2, norm_post2, w_in, hg_lb, hg_onorm, gla_w_gk, gla_b_gk, gla_onorm, w_br_hg, w_br_gla, w_out, w_ff_gate, w_ff_up, w_ff_down, loss_target, m_c_ctx, m_w_mod, m_b_mod, m_norm_pre1, m_norm_post1, m_norm_pre2, m_norm_post2, m_w_in, m_hg_lb, m_hg_onorm, m_gla_w_gk, m_gla_b_gk, m_gla_onorm, m_w_br_hg, m_w_br_gla, m_w_out, m_w_ff_gate, m_w_ff_up, m_w_ff_down, v_c_ctx, v_w_mod, v_b_mod, v_norm_pre1, v_norm_post1, v_norm_pre2, v_norm_post2, v_w_in, v_hg_lb, v_hg_onorm, v_gla_w_gk, v_gla_b_gk, v_gla_onorm, v_w_br_hg, v_w_br_gla, v_w_out, v_w_ff_gate, v_w_ff_up, v_w_ff_down):
    seq, d = x.shape[1], x.shape[2]
    ctx_len = ctx.shape[1]
    assert seq % TM == 0 and ctx_len % TM == 0 and d == 2 * HW
    ax, ay, ac = lax.axis_index("x"), lax.axis_index("y"), lax.axis_index("c")
    chip = 2 * ax + ay
    dev = 2 * chip + ac
    c_arr = jnp.reshape(ac, (1,)).astype(jnp.int32)
    chip_arr = jnp.reshape(chip, (1,)).astype(jnp.int32)
    transposed = ("w_in", "w_ff_gate", "w_ff_up")
    view = lambda a, nm: a[0].T if nm in transposed else a[0]

    sems_in, lands_in, token_in0 = _blocks_start([_cast_into_blocks(chip_arr, view(w_in, "w_in"), "cast_w_in")],
                                                 "gather_w_in_start")

    nc = d // 128
    pad8 = lambda a: jnp.pad(a, ((0, -a.shape[0] % 8), (0, 0)))
    small1 = jnp.concatenate([c.reshape(nc, 128) + token_in0[0, 0], pad8(hg_lb.reshape(4, 128)),
                              gla_w_gk.reshape(2 * RANK, 128), pad8(gla_b_gk.reshape(2, 128))], axis=0)
    blocks = [_cast_into_blocks(chip_arr, view(w_, nm), "cast_" + nm) for w_, nm in (
        (w_br_hg, "w_br_hg"), (w_br_gla, "w_br_gla"), (w_out, "w_out"), (w_ff_gate, "w_ff_gate"),
        (w_ff_up, "w_ff_up"), (w_ff_down, "w_ff_down"))]
    got1 = _allgather8(small1, "gather_small_params", after=blocks)
    c_all = got1[:, :nc, :].reshape(N_DEV, d)
    per_chip = got1[0::2]
    lb_full = per_chip[:, nc:nc + 4, :].transpose(1, 0, 2).reshape(4, HW)
    wgk_full = per_chip[:, nc + 8:nc + 8 + 2 * RANK, :].transpose(1, 0, 2).reshape(2, RANK, HW)
    bgk_full = per_chip[:, nc + 8 + 2 * RANK:nc + 10 + 2 * RANK, :].transpose(1, 0, 2).reshape(2, HW)
    wgk_pad = [jnp.zeros((128, HW), F32).at[dd * RANK:(dd + 1) * RANK].set(wgk_full[dd]) for dd in range(2)]
    bgk = [bgk_full[dd:dd + 1] for dd in range(2)]

    n_mod_cols = w_mod.shape[2]
    cond = jnp.concatenate([c_all, pad8(c_ctx.reshape(1, d))], axis=0)
    b_cols = lax.dynamic_slice(b_mod, (0, chip * n_mod_cols), (1, n_mod_cols))
    lands_in = _blocks_wait(sems_in, lands_in, [got1], "gather_w_in_wait")
    fwd_sems, lands_in, fwd_token = _forward_start(lands_in, "gather_w_in_forward_start")
    mod_part = _mod_forward(cond + fwd_token[0, 0], w_mod[0], b_cols, "mod_forward")
    mod_got = _allgather8(mod_part, "gather_mod")
    mod_all = mod_got[0::2].transpose(1, 0, 2).reshape(16, N_CHIP * n_mod_cols)
    modx = pad8(lax.dynamic_slice(mod_all, (dev, 0), (1, N_MOD * d)).reshape(N_MOD, d))
    modc = pad8(mod_all[8].reshape(N_MOD, d))

    gathered_in = _forward_wait(fwd_sems, lands_in, [mod_got], "gather_w_in_forward_wait")
    sems, lands, token = _blocks_start(blocks, "gather_rest_start", after=[gathered_in[0]])
    w_in_r = gathered_in[0].reshape(-1, d)

    norms = jnp.concatenate([norm_pre1, norm_post1, norm_pre2, norm_post2, jnp.zeros((4, d), F32)], axis=0)
    onorms = jnp.zeros((8, d), F32).at[0, :HD].set(hg_onorm[0]).at[1, :HD].set(gla_onorm[0])
    gla_side = [(wgk_pad[dd], bgk[dd]) for dd in range(2)]
    modx = modx + token[0, 0]
    front = _sample_front(x[0], ctx[0], modc, modx, norm_pre1, lb_full, gla_side, w_in_r)
    lands = _blocks_wait(sems, lands, front["o_list"], "gather_rest_wait")
    gathered = _blocks_finish(lands[:3], "gather_mix_finish")
    wbh, wbg = _unblocked(gathered[0]), _unblocked(gathered[1])
    wout = gathered[2].reshape(d, d)
    ffn_sems, ffn_lands, ffn_token = _forward_start(lands[3:], "gather_ffn_forward_start")
    onorms = onorms + ffn_token[0, 0]

    def ffn_weights(after):
        got = _forward_wait(ffn_sems, ffn_lands, after, "gather_ffn_forward_wait")
        return tuple(g.reshape(-1, d) for g in got)

    dff = w_ff_down.shape[1] * N_CHIP
    groups = {"ffn": ["w_ff_gate", "w_ff_up", "w_ff_down"], "mix": ["w_br_hg", "w_br_gla", "w_out"], "in": ["w_in"]}
    row_sharded = {"w_out": d // N_CHIP, "w_ff_down": dff // N_CHIP, "w_ff_gate": dff // N_CHIP,
                   "w_ff_up": dff // N_CHIP, "w_in": w_in.shape[2]}
    in_flight, to_sibling, small = {}, {}, {}

    def reduce_small(stats):
        small2 = jnp.concatenate([
            stats["stat_in"], stats["stat_mix"], stats["stat_ffn"],
            jnp.concatenate(stats["dlb"], axis=1), jnp.concatenate(stats["dbgk"], axis=1),
            jnp.concatenate([stats["dwgk"][0][0:RANK], stats["dwgk"][1][RANK:2 * RANK]], axis=1)], axis=0)
        assert small2.shape[0] == SMALL_ROWS
        got2 = _allgather8(small2, "gather_small_grads")
        total, dmod_all, g_b_mod, g_lb_full = _reduce_small(got2, lb_full, "reduce_small")
        dmod_cols = lax.dynamic_slice(dmod_all, (0, chip * n_mod_cols), (16, n_mod_cols))
        g_w_mod, cctx_part = _mod_backward(cond, w_mod[0], dmod_cols, "mod_backward")
        got3 = _allgather8(cctx_part, "gather_c_ctx_grad")
        g_c_ctx = _c_ctx_grad(got3, c_ctx.reshape(1, d), "c_ctx_grad")
        small.update(total=total, g_b_mod=g_b_mod, g_lb_full=g_lb_full, g_w_mod=g_w_mod, g_c_ctx=g_c_ctx)

    def reduce(group, grads):
        if group == "small":
            return reduce_small(grads)
        if group.startswith("push_"):
            return push(group[5:], grads)
        nms = groups[group]
        full = [g.reshape(N_CHIP, row_sharded[nm], d) if nm in row_sharded else _blocked(g, N_CHIP)
                for g, nm in zip(grads, nms)]
        sems_, full, lands_, token_ = _send_half_start(full, "grads_to_sibling_start_" + group)
        to_sibling[group] = (sems_, full, lands_)
        return token_[0, 0]

    def push(group, after):
        nms = groups[group]
        sems_, full, lands_ = to_sibling[group]
        if group == "in":
            after = list(after) + [small["g_c_ctx"], small["total"]]
        full, from_sibling = _send_half_wait(sems_, full, lands_, after, "grads_to_sibling_wait_" + group)
        pairs = [_pair_sum(c_arr, f, r_, "pair_sum_" + nm) for f, r_, nm in zip(full, from_sibling, nms)]
        after = [small["g_c_ctx"], small["total"]] if group == "in" else []
        sems_, pairs, lands_, token_ = _scatter_start(pairs, "grads_to_owner_start_" + group, after)
        in_flight[group] = (sems_, pairs, lands_, token_)
        return token_[0, 0]

    r = _sample_back(reduce, front, x[0], ctx[0], loss_target[0], modc, modx, norm_pre1, norms, onorms, lb_full,
                     gla_side, w_in_r, wbh, wbg, wout, ffn_weights)
    grad_x = r["grad_x"]

    weights = dict(w_in=(w_in, m_w_in, v_w_in), w_br_hg=(w_br_hg, m_w_br_hg, v_w_br_hg),
                   w_br_gla=(w_br_gla, m_w_br_gla, v_w_br_gla), w_out=(w_out, m_w_out, v_w_out),
                   w_ff_gate=(w_ff_gate, m_w_ff_gate, v_w_ff_gate), w_ff_up=(w_ff_up, m_w_ff_up, v_w_ff_up),
                   w_ff_down=(w_ff_down, m_w_ff_down, v_w_ff_down))
    names = ["w_in", "w_br_hg", "w_br_gla", "w_out", "w_ff_gate", "w_ff_up", "w_ff_down"]
    big, swapping = {}, {}

    def sum_and_swap(group, after):
        sems_, pairs, lands_, _ = in_flight[group]
        pairs, lands_ = _scatter_wait(sems_, pairs, lands_, after, "grads_to_owner_wait_" + group)
        own_half = [_sum_owner(chip_arr, pr, g, "chip_sum_" + nm) for pr, g, nm in zip(pairs, lands_, groups[group])]
        swapping[group] = _swap_start(own_half, "halves_to_sibling_start_" + group)
        return own_half[-1]

    def update(group, after):
        sems_, own_half, lands_ = swapping[group]
        own_half, other_half = _swap_wait(sems_, own_half, lands_, after, "halves_to_sibling_wait_" + group)
        done = []
        for nm, own, oth in zip(groups[group], own_half, other_half):
            w_, m_, v_ = (view(a, nm) for a in weights[nm])
            res = _adamw_halves(c_arr, own, oth, w_, m_, v_, "adamw_" + nm)
            big[nm] = [r_.T[None] if nm in transposed else r_[None] for r_ in res]
            done.append(res[1])
        return done

    token_in = in_flight["in"][3]
    summed_ffn = sum_and_swap("ffn", [token_in])
    summed_mix = sum_and_swap("mix", [summed_ffn])

    total, g_b_mod, g_lb_full, g_w_mod, g_c_ctx = (small[k] for k in ("total", "g_b_mod", "g_lb_full", "g_w_mod",
                                                                      "g_c_ctx"))
    g_pre1, g_post1, g_pre2, g_post2 = (total[r_:r_ + 1] for r_ in (ROW_PRE1, ROW_POST1, ROW_PRE2, ROW_POST2))
    g_hg_on, g_gla_on = total[ROW_ONORM:ROW_ONORM + 1, 0:HD], total[ROW_ONORM:ROW_ONORM + 1, HD:2 * HD]
    n_lb = hg_lb.shape[2]
    g_hg_lb = lax.dynamic_slice(g_lb_full, (0, chip * n_lb), (4, n_lb))
    g_bgk = lax.dynamic_slice(total[ROW_BGK:ROW_BGK + 1].reshape(2, HW), (0, chip * n_lb), (2, n_lb))
    g_wgk_full = total[ROW_WGK:ROW_WGK + RANK].reshape(RANK, 2, HW).transpose(1, 0, 2).reshape(2 * RANK, HW)
    g_wgk = lax.dynamic_slice(g_wgk_full, (0, chip * n_lb), (2 * RANK, n_lb))

    small_items = [
        (g_c_ctx, c_ctx.reshape(1, d), m_c_ctx.reshape(1, d), v_c_ctx.reshape(1, d)),
        (g_b_mod, b_mod, m_b_mod, v_b_mod),
        (g_pre1, norm_pre1, m_norm_pre1, v_norm_pre1),
        (g_post1, norm_post1, m_norm_post1, v_norm_post1),
        (g_pre2, norm_pre2, m_norm_pre2, v_norm_pre2),
        (g_post2, norm_post2, m_norm_post2, v_norm_post2),
        (g_hg_lb, hg_lb.reshape(4, n_lb), m_hg_lb.reshape(4, n_lb), v_hg_lb.reshape(4, n_lb)),
        (g_hg_on, hg_onorm, m_hg_onorm, v_hg_onorm),
        (g_wgk, gla_w_gk.reshape(2 * RANK, n_lb), m_gla_w_gk.reshape(2 * RANK, n_lb), v_gla_w_gk.reshape(2 * RANK, n_lb)),
        (g_bgk, gla_b_gk.reshape(2, n_lb), m_gla_b_gk.reshape(2, n_lb), v_gla_b_gk.reshape(2, n_lb)),
        (g_gla_on, gla_onorm, m_gla_onorm, v_gla_onorm),
    ]
    small_res = _adamw_whole(small_items, "adamw_small")
    mod_res = _adamw_tiled(g_w_mod, w_mod[0], m_w_mod[0], v_w_mod[0], "adamw_w_mod")
    done_ffn = update("ffn", [summed_mix, mod_res[0], small_res[0][0]])
    done_mix = update("mix", done_ffn)
    update("in", [sum_and_swap("in", done_mix)])

    loss = total[ROW_LOSS, 0]

    shapes = dict(c_ctx=c_ctx.shape, b_mod=b_mod.shape, norm_pre1=norm_pre1.shape, norm_post1=norm_post1.shape,
                  norm_pre2=norm_pre2.shape, norm_post2=norm_post2.shape, hg_lb=hg_lb.shape, hg_onorm=hg_onorm.shape,
                  gla_w_gk=gla_w_gk.shape, gla_b_gk=gla_b_gk.shape, gla_onorm=gla_onorm.shape)
    small_names = ["c_ctx", "b_mod", "norm_pre1", "norm_post1", "norm_pre2", "norm_post2", "hg_lb", "hg_onorm",
                   "gla_w_gk", "gla_b_gk", "gla_onorm"]
    grads, deltas, new_m, new_v = {}, {}, {}, {}
    for nm, item, res in zip(small_names, small_items, small_res):
        grads[nm] = item[0].reshape(shapes[nm])
        deltas[nm], new_m[nm], new_v[nm] = (r_.reshape(shapes[nm]) for r_ in res)
    grads["w_mod"] = g_w_mod[None]
    deltas["w_mod"], new_m["w_mod"], new_v["w_mod"] = (r_[None] for r_ in mod_res)
    for nm in names:
        grads[nm], deltas[nm], new_m[nm], new_v[nm] = big[nm]
    order = ["c_ctx", "w_mod", "b_mod", "norm_pre1", "norm_post1", "norm_pre2", "norm_post2", "w_in", "hg_lb",
             "hg_onorm", "gla_w_gk", "gla_b_gk", "gla_onorm", "w_br_hg", "w_br_gla", "w_out", "w_ff_gate", "w_ff_up",
             "w_ff_down"]
    return (loss, grad_x[None], *[grads[n] for n in order], *[deltas[n] for n in order],
            *[new_m[n] for n in order], *[new_v[n] for n in order])
```

```python
import functools

import jax
import jax.numpy as jnp
from jax import lax
from jax.experimental import pallas as pl
from jax.experimental.pallas import tpu as pltpu

F32 = jnp.float32
BF16 = jnp.bfloat16
MESH = pl.DeviceIdType.MESH

EPS = 1e-6
CHUNK = 64
SUB = 16
NSUB = CHUNK // SUB
NH = 4
HD = 128
HW = NH * HD
RANK = 16
GATE_NORM = 16.0
N_MOD = 6
TM = 256
BWD_LANES = 8
N_DEV = 8
N_CHIP = 4
VMEM_LIMIT = 56 * 1024 * 1024

ADAM_LR = 0.001
ADAM_B1 = 0.9
ADAM_B2 = 0.999
ADAM_EPS = 1e-08
ADAM_WD = 0.01
ADAM_STEP = 10

VMEM_SPEC = pl.BlockSpec(memory_space=pltpu.VMEM)
ANY_SPEC = pl.BlockSpec(memory_space=pl.ANY)
HBM_SPEC = pl.BlockSpec(memory_space=pltpu.HBM)
SEM_SPEC = pl.BlockSpec(memory_space=pltpu.SEMAPHORE)
EFFECT = pltpu.SideEffectType.DATAFLOW_SIDE_EFFECTING


def _cparams(**kw):
    return pltpu.CompilerParams(vmem_limit_bytes=VMEM_LIMIT, **kw)


def _dot(a, b):
    return jnp.dot(a.astype(BF16), b.astype(BF16), preferred_element_type=F32)


def _dot_nt(a, b):
    return lax.dot_general(a.astype(BF16), b.astype(BF16), (((1,), (1,)), ((), ())), preferred_element_type=F32)


def _dot_tn(a, b):
    return lax.dot_general(a.astype(BF16), b.astype(BF16), (((0,), (0,)), ((), ())), preferred_element_type=F32)


def _sigmoid(x):
    return 1.0 / (1.0 + jnp.exp(-x))


def _silu(x):
    return x * _sigmoid(x)


def _dsilu(x):
    s = _sigmoid(x)
    return s * (1.0 + x * (1.0 - s))


def _log_sigmoid(x):
    return jnp.minimum(x, 0.0) - jnp.log(1.0 + jnp.exp(-jnp.abs(x)))


def _colsum(a):
    return jnp.sum(a, axis=0, keepdims=True)


def _rms(a):
    r = lax.rsqrt(jnp.mean(a * a, axis=-1, keepdims=True) + EPS)
    return a * r, r


def _rms_bwd(dn, n, r):
    return r * (dn - n * jnp.mean(dn * n, axis=-1, keepdims=True))


def _place():
    x, y, c = lax.axis_index("x"), lax.axis_index("y"), lax.axis_index("c")
    chips = [(1 - x, y), (x, 1 - y), (1 - x, 1 - y)]
    return x, y, c, chips


def _allgather8(v, name, after=()):
    rows, cols = v.shape
    n_after = len(after)

    def body(x_ref, *rest):
        out_ref, send_sems, recv_sems, local_sem = rest[n_after:]
        x, y, c, chips = _place()
        me, sibling = (x, y, c), (x, y, 1 - c)

        def blk(px, py, pc):
            return out_ref.at[4 * px + 2 * py + pc]

        def copy(k, block, to, src=None):
            return pltpu.make_async_remote_copy(
                src_ref=blk(*block) if src is None else src, dst_ref=blk(*block),
                send_sem=send_sems.at[k], recv_sem=recv_sems.at[k], device_id=to, device_id_type=MESH)

        mine = pltpu.make_async_copy(x_ref, blk(*me), local_sem)
        mine.start()
        first = [copy(0, me, sibling, src=x_ref)]
        first += [copy(1 + j, me, (*chip, c), src=x_ref) for j, chip in enumerate(chips)]
        for cp in first:
            cp.start()
        passed = [copy(4 + j, (*chip, c), sibling) for j, chip in enumerate(chips)]
        for j, chip in enumerate(chips):
            copy(1 + j, (*chip, c), me).wait_recv()
            passed[j].start()
        copy(0, sibling, me).wait_recv()
        for j, chip in enumerate(chips):
            copy(4 + j, (*chip, 1 - c), me).wait_recv()
        for cp in first + passed:
            cp.wait_send()
        mine.wait()

    return pl.pallas_call(
        body, name=name,
        out_shape=jax.ShapeDtypeStruct((N_DEV, rows, cols), v.dtype),
        in_specs=[VMEM_SPEC] + [ANY_SPEC] * n_after, out_specs=VMEM_SPEC,
        scratch_shapes=[pltpu.SemaphoreType.DMA((7,)), pltpu.SemaphoreType.DMA((7,)), pltpu.SemaphoreType.DMA],
    )(v, *after)


def _cast_into_blocks(chip_arr, w, name):
    rows, cols = w.shape
    tr = _row_tile(rows, 16, 256)

    def body(chip_ref, w_ref, o_ref):
        o_ref[0] = w_ref[...].astype(BF16)

    return pl.pallas_call(
        body, name=name,
        grid_spec=pltpu.PrefetchScalarGridSpec(
            num_scalar_prefetch=1, grid=(rows // tr,),
            in_specs=[pl.BlockSpec((tr, cols), lambda i, chip_ref: (i, 0))],
            out_specs=pl.BlockSpec((1, tr, cols), lambda i, chip_ref: (chip_ref[0], i, 0))),
        out_shape=jax.ShapeDtypeStruct((N_CHIP, rows, cols), BF16),
        compiler_params=_cparams(dimension_semantics=("parallel",)),
    )(chip_arr, w)


def _halved_by_rows(shape):
    return (shape[1] // 2) % 16 == 0


def _half_of(ref, pc, block=None):
    lead = slice(None) if block is None else block
    if _halved_by_rows(ref.shape):
        h = ref.shape[1] // 2
        return ref.at[lead, pl.ds(pl.multiple_of(pc * h, 16), h), :]
    h = ref.shape[2] // 2
    return ref.at[lead, :, pl.ds(pl.multiple_of(pc * h, 128), h)]


def _half_shape(shape):
    return (shape[0], shape[1] // 2, shape[2]) if _halved_by_rows(shape) else (shape[0], shape[1], shape[2] // 2)


def _half_rows(ref, chip_id, pc):
    return _half_of(ref, pc, chip_id)


def _hbm(a):
    return pltpu.with_memory_space_constraint(a, pltpu.HBM)


def _blocks_start(lands, name, after=()):
    n = len(lands)
    n_sem = 3 * n
    first = n + len(after)

    def body(*refs):
        lnd = refs[:n]
        send_sems, recv_sems = refs[first:first + n_sem], refs[first + n_sem:first + 2 * n_sem]
        token = refs[-1]
        x, y, c, chips = _place()
        me_chip = 2 * x + y
        for k in range(n):
            for j, chip in enumerate(chips):
                pltpu.make_async_remote_copy(
                    src_ref=_half_rows(lnd[k], me_chip, c), dst_ref=_half_rows(lnd[k], me_chip, c),
                    send_sem=send_sems[3 * k + j], recv_sem=recv_sems[3 * k + j],
                    device_id=(*chip, c), device_id_type=MESH).start()
        token[...] = jnp.zeros_like(token)

    out = pl.pallas_call(
        body, name=name,
        out_shape=(*[pltpu.SemaphoreType.DMA(())] * (2 * n_sem),
                   *[pltpu.HBM(l.shape, l.dtype) for l in lands],
                   jax.ShapeDtypeStruct((8, 128), F32)),
        in_specs=[HBM_SPEC] * n + [ANY_SPEC] * len(after),
        out_specs=(*[SEM_SPEC] * (2 * n_sem), *[HBM_SPEC] * n, VMEM_SPEC),
        input_output_aliases={i: 2 * n_sem + i for i in range(n)},
        compiler_params=pltpu.CompilerParams(has_side_effects=EFFECT),
    )(*[_hbm(l) for l in lands], *after)
    return list(out[:2 * n_sem]), list(out[2 * n_sem:2 * n_sem + n]), out[-1]


def _blocks_wait(sems, lands, after, name):
    n = len(lands)
    n_sem = 3 * n

    def body(*refs):
        lnd = refs[:n]
        s_sems, r_sems = refs[n:n + n_sem], refs[n + n_sem:n + 2 * n_sem]
        x, y, c, chips = _place()
        me_chip = 2 * x + y
        for k in range(n):
            for j, (px, py) in enumerate(chips):
                cp = pltpu.make_async_remote_copy(
                    src_ref=_half_rows(lnd[k], me_chip, c), dst_ref=_half_rows(lnd[k], 2 * px + py, c),
                    send_sem=s_sems[3 * k + j], recv_sem=r_sems[3 * k + j],
                    device_id=(px, py, c), device_id_type=MESH)
                cp.wait_send()
                cp.wait_recv()

    out = pl.pallas_call(
        body, name=name,
        out_shape=tuple(pltpu.HBM(l.shape, l.dtype) for l in lands),
        in_specs=[HBM_SPEC] * n + [SEM_SPEC] * (2 * n_sem) + [ANY_SPEC] * len(after),
        out_specs=[HBM_SPEC] * n,
        input_output_aliases={i: i for i in range(n)},
        compiler_params=pltpu.CompilerParams(has_side_effects=EFFECT),
    )(*lands, *sems, *after)
    return list(out)


def _forward_start(lands, name):
    n = len(lands)
    n_sem = 3 * n

    def body(*refs):
        lnd = refs[:n]
        send_sems, recv_sems = refs[n:n + n_sem], refs[n + n_sem:n + 2 * n_sem]
        x, y, c, chips = _place()
        for k in range(n):
            for j, (px, py) in enumerate(chips):
                pltpu.make_async_remote_copy(
                    src_ref=_half_rows(lnd[k], 2 * px + py, c), dst_ref=_half_rows(lnd[k], 2 * px + py, c),
                    send_sem=send_sems[3 * k + j], recv_sem=recv_sems[3 * k + j],
                    device_id=(x, y, 1 - c), device_id_type=MESH).start()
        refs[-1][...] = jnp.zeros_like(refs[-1])

    out = pl.pallas_call(
        body, name=name,
        out_shape=(*[pltpu.SemaphoreType.DMA(())] * (2 * n_sem), *[pltpu.HBM(l.shape, l.dtype) for l in lands],
                   jax.ShapeDtypeStruct((8, 128), F32)),
        in_specs=[HBM_SPEC] * n,
        out_specs=(*[SEM_SPEC] * (2 * n_sem), *[HBM_SPEC] * n, VMEM_SPEC),
        input_output_aliases={i: 2 * n_sem + i for i in range(n)},
        compiler_params=pltpu.CompilerParams(has_side_effects=EFFECT),
    )(*[_hbm(l) for l in lands])
    return list(out[:2 * n_sem]), list(out[2 * n_sem:2 * n_sem + n]), out[-1]


def _forward_wait(sems, lands, after, name):
    n = len(lands)
    n_sem = 3 * n

    def body(*refs):
        lnd = refs[:n]
        s_sems, r_sems = refs[n:n + n_sem], refs[n + n_sem:n + 2 * n_sem]
        x, y, c, chips = _place()
        for k in range(n):
            for j, (px, py) in enumerate(chips):
                cp = pltpu.make_async_remote_copy(
                    src_ref=_half_rows(lnd[k], 2 * px + py, c), dst_ref=_half_rows(lnd[k], 2 * px + py, 1 - c),
                    send_sem=s_sems[3 * k + j], recv_sem=r_sems[3 * k + j],
                    device_id=(x, y, 1 - c), device_id_type=MESH)
                cp.wait_send()
                cp.wait_recv()

    out = pl.pallas_call(
        body, name=name,
        out_shape=tuple(pltpu.HBM(l.shape, l.dtype) for l in lands),
        in_specs=[HBM_SPEC] * n + [SEM_SPEC] * (2 * n_sem) + [ANY_SPEC] * len(after),
        out_specs=[HBM_SPEC] * n,
        input_output_aliases={i: i for i in range(n)},
        compiler_params=pltpu.CompilerParams(has_side_effects=EFFECT),
    )(*lands, *sems, *after)
    return list(out)


def _blocks_finish(lands, name):
    n = len(lands)

    def body(*refs):
        lnd = refs[n:2 * n]
        send_sems, recv_sems = refs[2 * n:]
        x, y, c, chips = _place()
        sibling = (x, y, 1 - c)

        def copy(k, j, chip_id, pc):
            return pltpu.make_async_remote_copy(
                src_ref=_half_rows(lnd[k], chip_id, pc), dst_ref=_half_rows(lnd[k], chip_id, pc),
                send_sem=send_sems.at[k, j], recv_sem=recv_sems.at[k, j], device_id=sibling, device_id_type=MESH)

        started = []
        for k in range(n):
            for j, (px, py) in enumerate(chips):
                cp = copy(k, j, 2 * px + py, c)
                cp.start()
                started.append(cp)
        for k in range(n):
            for j, (px, py) in enumerate(chips):
                copy(k, j, 2 * px + py, 1 - c).wait_recv()
        for cp in started:
            cp.wait_send()

    out = pl.pallas_call(
        body, name=name,
        out_shape=[jax.ShapeDtypeStruct(l.shape, l.dtype) for l in lands],
        in_specs=[ANY_SPEC] * n, out_specs=[ANY_SPEC] * n,
        input_output_aliases={i: i for i in range(n)},
        scratch_shapes=[pltpu.SemaphoreType.DMA((n, 3)), pltpu.SemaphoreType.DMA((n, 3))],
    )(*lands)
    return list(out)


def _send_half_start(arrs, name):
    n = len(arrs)

    def body(*refs):
        ins, lnd = refs[:n], refs[n:2 * n]
        send_sems, recv_sems = refs[2 * n:3 * n], refs[3 * n:4 * n]
        token = refs[-1]
        x, y, c, _ = _place()
        for k in range(n):
            pltpu.make_async_remote_copy(
                src_ref=_half_of(ins[k], 1 - c), dst_ref=lnd[k], send_sem=send_sems[k], recv_sem=recv_sems[k],
                device_id=(x, y, 1 - c), device_id_type=MESH).start()
        token[...] = jnp.zeros_like(token)

    lands = [_hbm(lax.empty(_half_shape(a.shape), a.dtype)) for a in arrs]
    out = pl.pallas_call(
        body, name=name,
        out_shape=(*[pltpu.SemaphoreType.DMA(())] * (2 * n), *[pltpu.HBM(a.shape, a.dtype) for a in arrs],
                   *[pltpu.HBM(l.shape, l.dtype) for l in lands], jax.ShapeDtypeStruct((8, 128), F32)),
        in_specs=[HBM_SPEC] * (2 * n),
        out_specs=(*[SEM_SPEC] * (2 * n), *[HBM_SPEC] * (2 * n), VMEM_SPEC),
        input_output_aliases={i: 2 * n + i for i in range(2 * n)},
        compiler_params=pltpu.CompilerParams(has_side_effects=EFFECT),
    )(*[_hbm(a) for a in arrs], *lands)
    return list(out[:2 * n]), list(out[2 * n:3 * n]), list(out[3 * n:4 * n]), out[-1]


def _send_half_wait(sems, arrs, lands, after, name):
    n = len(arrs)

    def body(*refs):
        ins, lnd = refs[:n], refs[n:2 * n]
        s_sems, r_sems = refs[2 * n:3 * n], refs[3 * n:4 * n]
        x, y, c, _ = _place()
        for k in range(n):
            cp = pltpu.make_async_remote_copy(
                src_ref=_half_of(ins[k], 1 - c), dst_ref=lnd[k], send_sem=s_sems[k], recv_sem=r_sems[k],
                device_id=(x, y, 1 - c), device_id_type=MESH)
            cp.wait_send()
            cp.wait_recv()

    out = pl.pallas_call(
        body, name=name,
        out_shape=tuple(pltpu.HBM(a.shape, a.dtype) for a in list(arrs) + list(lands)),
        in_specs=[HBM_SPEC] * (2 * n) + [SEM_SPEC] * (2 * n) + [ANY_SPEC] * len(after),
        out_specs=[HBM_SPEC] * (2 * n),
        input_output_aliases={i: i for i in range(2 * n)},
        compiler_params=pltpu.CompilerParams(has_side_effects=EFFECT),
    )(*arrs, *lands, *sems, *after)
    return list(out[:n]), list(out[n:])


def _scatter_start(arrs, name, after=()):
    n = len(arrs)
    n_sem = 3 * n
    first = 2 * n + len(after)

    def body(*refs):
        ins, lnd = refs[:n], refs[n:2 * n]
        send_sems, recv_sems = refs[first:first + n_sem], refs[first + n_sem:first + 2 * n_sem]
        token = refs[-1]
        x, y, c, chips = _place()
        me_chip = 2 * x + y
        for k in range(n):
            for j, (px, py) in enumerate(chips):
                pltpu.make_async_remote_copy(
                    src_ref=ins[k].at[2 * px + py], dst_ref=lnd[k].at[me_chip],
                    send_sem=send_sems[3 * k + j], recv_sem=recv_sems[3 * k + j],
                    device_id=(px, py, c), device_id_type=MESH).start()
        token[...] = jnp.zeros_like(token)

    lands = [_hbm(lax.empty(a.shape, a.dtype)) for a in arrs]
    out = pl.pallas_call(
        body, name=name,
        out_shape=(*[pltpu.SemaphoreType.DMA(())] * (2 * n_sem),
                   *[pltpu.HBM(a.shape, a.dtype) for a in arrs], *[pltpu.HBM(a.shape, a.dtype) for a in arrs],
                   jax.ShapeDtypeStruct((8, 128), F32)),
        in_specs=[HBM_SPEC] * (2 * n) + [ANY_SPEC] * len(after),
        out_specs=(*[SEM_SPEC] * (2 * n_sem), *[HBM_SPEC] * (2 * n), VMEM_SPEC),
        input_output_aliases={i: 2 * n_sem + i for i in range(2 * n)},
        compiler_params=pltpu.CompilerParams(has_side_effects=EFFECT),
    )(*[_hbm(a) for a in arrs], *lands, *after)
    base = 2 * n_sem
    return list(out[:base]), list(out[base:base + n]), list(out[base + n:base + 2 * n]), out[-1]


def _scatter_wait(sems, arrs, lands, after, name):
    n = len(arrs)
    n_sem = 3 * n

    def body(*refs):
        ins, lnd = refs[:n], refs[n:2 * n]
        s_sems, r_sems = refs[2 * n:2 * n + n_sem], refs[2 * n + n_sem:2 * n + 2 * n_sem]
        x, y, c, chips = _place()
        for k in range(n):
            for j, (px, py) in enumerate(chips):
                cp = pltpu.make_async_remote_copy(
                    src_ref=ins[k].at[2 * px + py], dst_ref=lnd[k].at[2 * px + py],
                    send_sem=s_sems[3 * k + j], recv_sem=r_sems[3 * k + j],
                    device_id=(px, py, c), device_id_type=MESH)
                cp.wait_send()
                cp.wait_recv()

    out = pl.pallas_call(
        body, name=name,
        out_shape=tuple(pltpu.HBM(a.shape, a.dtype) for a in list(arrs) + list(lands)),
        in_specs=[HBM_SPEC] * (2 * n) + [SEM_SPEC] * (2 * n_sem) + [ANY_SPEC] * len(after),
        out_specs=[HBM_SPEC] * (2 * n),
        input_output_aliases={i: i for i in range(2 * n)},
        compiler_params=pltpu.CompilerParams(has_side_effects=EFFECT),
    )(*arrs, *lands, *sems, *after)
    return list(out[:n]), list(out[n:])


def _sum_owner(chip_arr, pairs, got, name):
    nb, h, cols = got.shape
    tr = _row_tile(h, 16, 256)

    def body(chip_ref, own_ref, a_ref, b_ref, c_ref, o_ref):
        o_ref[...] = ((own_ref[0].astype(F32) + a_ref[0].astype(F32)) + b_ref[0].astype(F32)) + c_ref[0].astype(F32)

    def slot(off):
        return pl.BlockSpec((1, tr, cols), lambda i, chip_ref: ((chip_ref[0] + off) % N_CHIP, i, 0))

    return pl.pallas_call(
        body, name=name,
        grid_spec=pltpu.PrefetchScalarGridSpec(
            num_scalar_prefetch=1, grid=(h // tr,),
            in_specs=[slot(0), slot(1), slot(2), slot(3)],
            out_specs=pl.BlockSpec((tr, cols), lambda i, chip_ref: (i, 0))),
        out_shape=jax.ShapeDtypeStruct((h, cols), F32),
        compiler_params=_cparams(dimension_semantics=("parallel",)),
    )(chip_arr, pairs, got, got, got)


def _swap_start(arrs, name, after=()):
    n = len(arrs)
    first = 2 * n + len(after)

    def body(*refs):
        ins, lnd = refs[:n], refs[n:2 * n]
        send_sems, recv_sems = refs[first:first + n], refs[first + n:first + 2 * n]
        x, y, c, _ = _place()
        for k in range(n):
            pltpu.make_async_remote_copy(
                src_ref=ins[k], dst_ref=lnd[k], send_sem=send_sems[k], recv_sem=recv_sems[k],
                device_id=(x, y, 1 - c), device_id_type=MESH).start()

    lands = [_hbm(lax.empty(a.shape, a.dtype)) for a in arrs]
    out = pl.pallas_call(
        body, name=name,
        out_shape=(*[pltpu.SemaphoreType.DMA(())] * (2 * n), *[pltpu.HBM(a.shape, a.dtype) for a in arrs],
                   *[pltpu.HBM(a.shape, a.dtype) for a in arrs]),
        in_specs=[HBM_SPEC] * (2 * n) + [ANY_SPEC] * len(after),
        out_specs=(*[SEM_SPEC] * (2 * n), *[HBM_SPEC] * (2 * n)),
        input_output_aliases={i: 2 * n + i for i in range(2 * n)},
        compiler_params=pltpu.CompilerParams(has_side_effects=EFFECT),
    )(*[_hbm(a) for a in arrs], *lands, *after)
    return list(out[:2 * n]), list(out[2 * n:3 * n]), list(out[3 * n:4 * n])


def _swap_wait(sems, arrs, lands, after, name):
    n = len(arrs)

    def body(*refs):
        ins, lnd = refs[:n], refs[n:2 * n]
        s_sems, r_sems = refs[2 * n:3 * n], refs[3 * n:4 * n]
        x, y, c, _ = _place()
        for k in range(n):
            cp = pltpu.make_async_remote_copy(
                src_ref=ins[k], dst_ref=lnd[k], send_sem=s_sems[k], recv_sem=r_sems[k],
                device_id=(x, y, 1 - c), device_id_type=MESH)
            cp.wait_send()
            cp.wait_recv()

    out = pl.pallas_call(
        body, name=name,
        out_shape=tuple(pltpu.HBM(a.shape, a.dtype) for a in list(arrs) + list(lands)),
        in_specs=[HBM_SPEC] * (2 * n) + [SEM_SPEC] * (2 * n) + [ANY_SPEC] * len(after),
        out_specs=[HBM_SPEC] * (2 * n),
        input_output_aliases={i: i for i in range(2 * n)},
        compiler_params=pltpu.CompilerParams(has_side_effects=EFFECT),
    )(*arrs, *lands, *sems, *after)
    return list(out[:n]), list(out[n:])


def _row_tile(h, mult=8, cap=128):
    for t in range(cap - cap % mult, mult - 1, -mult):
        if h % t == 0:
            return t
    if mult > 8:
        return _row_tile(h, 8, cap)
    raise ValueError(h)


def _pair_sum(c_arr, full, recv, name):
    nb, rows, cols = full.shape

    def body(c_ref, f_ref, r_ref, o_ref):
        o_ref[...] = (f_ref[...] + r_ref[...]).astype(BF16)

    if _halved_by_rows(full.shape):
        h = rows // 2
        tr = _row_tile(h, 16, 256)
        steps = h // tr
        own = pl.BlockSpec((1, tr, cols), lambda b, i, c_ref: (b, c_ref[0] * steps + i, 0))
        half = pl.BlockSpec((1, tr, cols), lambda b, i, c_ref: (b, i, 0))
    else:
        steps = 1
        own = pl.BlockSpec((1, rows, cols // 2), lambda b, i, c_ref: (b, 0, c_ref[0]))
        half = pl.BlockSpec((1, rows, cols // 2), lambda b, i, c_ref: (b, 0, 0))
    return pl.pallas_call(
        body, name=name,
        grid_spec=pltpu.PrefetchScalarGridSpec(
            num_scalar_prefetch=1, grid=(nb, steps), in_specs=[own, half], out_specs=half),
        out_shape=jax.ShapeDtypeStruct(_half_shape(full.shape), BF16),
        compiler_params=_cparams(dimension_semantics=("parallel", "parallel")),
    )(c_arr, full, recv)


def _adam_math(g, w, m, v):
    m1 = ADAM_B1 * m + (1.0 - ADAM_B1) * g
    v1 = ADAM_B2 * v + (1.0 - ADAM_B2) * (g * g)
    m_hat = m1 / (1.0 - ADAM_B1 ** ADAM_STEP)
    v_hat = v1 / (1.0 - ADAM_B2 ** ADAM_STEP)
    delta = -ADAM_LR * (m_hat / (jnp.sqrt(v_hat) + ADAM_EPS) + ADAM_WD * w)
    return delta, m1, v1


def _adamw_halves(c_arr, own, other, w, m, v, name):
    rows, cols = w.shape
    by_rows = own.shape[1] == cols

    def body(c_ref, own_ref, oth_ref, w_ref, m_ref, v_ref, g_out, d_out, m_out, v_out):
        if by_rows:
            g = jnp.where(pl.program_id(0) == c_ref[0], own_ref[...], oth_ref[...])
        else:
            own_, oth_ = own_ref[...], oth_ref[...]
            g = jnp.where(c_ref[0] == 0, jnp.concatenate([own_, oth_], axis=1), jnp.concatenate([oth_, own_], axis=1))
        d, m1, v1 = _adam_math(g, w_ref[...], m_ref[...], v_ref[...])
        g_out[...] = g
        d_out[...] = d
        m_out[...] = m1
        v_out[...] = v1

    if by_rows:
        h = rows // 2
        tr = _row_tile(h)
        steps = h // tr
        grid = (2, steps)
        half_spec = pl.BlockSpec((tr, cols), lambda p, i, c_ref: (i, 0))
        full_spec = pl.BlockSpec((tr, cols), lambda p, i, c_ref: (p * steps + i, 0))
    else:
        tr = _row_tile(rows)
        grid = (1, rows // tr)
        half_spec = pl.BlockSpec((tr, cols // 2), lambda p, i, c_ref: (i, 0))
        full_spec = pl.BlockSpec((tr, cols), lambda p, i, c_ref: (i, 0))
    return pl.pallas_call(
        body, name=name,
        grid_spec=pltpu.PrefetchScalarGridSpec(
            num_scalar_prefetch=1, grid=grid,
            in_specs=[half_spec, half_spec, full_spec, full_spec, full_spec],
            out_specs=[full_spec] * 4),
        out_shape=[jax.ShapeDtypeStruct(w.shape, F32)] * 4,
        compiler_params=_cparams(dimension_semantics=("parallel", "parallel")),
    )(c_arr, own, other, w, m, v)


def _adamw_whole(items, name):
    n = len(items)

    def body(*refs):
        ins, outs = refs[:4 * n], refs[4 * n:]
        for k in range(n):
            g, w, m, v = (r[...] for r in ins[4 * k:4 * k + 4])
            d, m1, v1 = _adam_math(g, w, m, v)
            outs[3 * k][...] = d
            outs[3 * k + 1][...] = m1
            outs[3 * k + 2][...] = v1

    flat = [a for it in items for a in it]
    shapes = [jax.ShapeDtypeStruct(it[1].shape, F32) for it in items for _ in range(3)]
    out = pl.pallas_call(
        body, name=name, out_shape=shapes,
        in_specs=[VMEM_SPEC] * (4 * n), out_specs=[VMEM_SPEC] * (3 * n),
        compiler_params=_cparams(),
    )(*flat)
    return [tuple(out[3 * k:3 * k + 3]) for k in range(n)]


def _adamw_tiled(g, w, m, v, name):
    rows, cols = w.shape
    tr = _row_tile(rows)

    def body(g_ref, w_ref, m_ref, v_ref, d_out, m_out, v_out):
        d, m1, v1 = _adam_math(g_ref[...], w_ref[...], m_ref[...], v_ref[...])
        d_out[...] = d
        m_out[...] = m1
        v_out[...] = v1

    spec = pl.BlockSpec((tr, cols), lambda i: (i, 0))
    return pl.pallas_call(
        body, name=name, grid=(rows // tr,),
        out_shape=[jax.ShapeDtypeStruct(w.shape, F32)] * 3,
        in_specs=[spec] * 4, out_specs=[spec] * 3,
        compiler_params=_cparams(dimension_semantics=("parallel",)),
    )(g, w, m, v)


def _mod_forward(cond, w_mod, b_mod_cols, name):
    def body(c_ref, w_ref, b_ref, o_ref):
        o_ref[...] = _dot(_silu(c_ref[...]), w_ref[...]) + b_ref[...]

    return pl.pallas_call(
        body, name=name, out_shape=jax.ShapeDtypeStruct((cond.shape[0], w_mod.shape[1]), F32),
        in_specs=[VMEM_SPEC] * 3, out_specs=VMEM_SPEC, compiler_params=_cparams(),
    )(cond, w_mod, b_mod_cols)


def _mod_backward(cond, w_mod, dmod_cols, name):
    def body(c_ref, w_ref, d_ref, gw_ref, gc_ref):
        s = _silu(c_ref[...])
        d = d_ref[...]
        gw_ref[...] = _dot_tn(s, d)
        gc_ref[...] = _dot_nt(d[8:16, :], w_ref[...])

    return pl.pallas_call(
        body, name=name,
        out_shape=[jax.ShapeDtypeStruct(w_mod.shape, F32), jax.ShapeDtypeStruct((8, w_mod.shape[0]), F32)],
        in_specs=[VMEM_SPEC] * 3, out_specs=[VMEM_SPEC] * 2, compiler_params=_cparams(),
    )(cond, w_mod, dmod_cols)


def _col_chunks(width, step=512):
    return [(s, min(step, width - s)) for s in range(0, width, step)]


def _w_in_row(p_off):
    if p_off < 9 * HW:
        return p_off
    return 9 * HW if p_off == OFF_LR else p_off + 2 * RANK


def _in_projection(ctx0, x0, modc, modx, pre1, w_t, n_ctx_tiles, name):
    d = x0.shape[1]
    rows = ctx0.shape[0] + x0.shape[0]
    width = P_WIDTH

    def body(ctx_ref, x_ref, modc_ref, modx_ref, pre_ref, w_ref, h_ref, p_ref):
        is_ctx = pl.program_id(0) < n_ctx_tiles
        n, _ = _rms(jnp.where(is_ctx, ctx_ref[...], x_ref[...]))
        shift = jnp.where(is_ctx, modc_ref[0:1, :], modx_ref[0:1, :])
        scale = jnp.where(is_ctx, modc_ref[1:2, :], modx_ref[1:2, :])
        h = (n * pre_ref[...] * (1.0 + scale) + shift).astype(BF16)
        h_ref[...] = h
        for s, w in _col_chunks(width):
            p_ref[:, s:s + w] = _dot_nt(h, w_ref[_w_in_row(s):_w_in_row(s) + w, :])

    row = lambda i: (i, 0)
    fixed = lambda i: (0, 0)
    return pl.pallas_call(
        body, name=name, grid=(rows // TM,),
        out_shape=[jax.ShapeDtypeStruct((rows, d), BF16), jax.ShapeDtypeStruct((rows, width), F32)],
        in_specs=[pl.BlockSpec((TM, d), lambda i: (jnp.minimum(i, n_ctx_tiles - 1), 0)),
                  pl.BlockSpec((TM, d), lambda i: (jnp.maximum(i - n_ctx_tiles, 0), 0)),
                  pl.BlockSpec((8, d), fixed), pl.BlockSpec((8, d), fixed), pl.BlockSpec((1, d), fixed), VMEM_SPEC],
        out_specs=[pl.BlockSpec((TM, d), row), pl.BlockSpec((TM, width), row)],
        compiler_params=_cparams(dimension_semantics=("parallel",)),
    )(ctx0, x0, modc, modx, pre1, w_t)


C_HQ, C_HI, C_HF_FW, C_HF_BW, C_HGATE, C_GQ, C_GK, C_GV, C_GGATE = range(9)
OFF_GATE_HG = 9 * HW
OFF_LR = 13 * HW
P_WIDTH = OFF_LR + 128


def _head_norm_fwd(o, w):
    outs, ns, rs = [], [], []
    for h in range(NH):
        n, r = _rms(o[:, h * HD:(h + 1) * HD])
        ns.append(n)
        rs.append(r)
        outs.append(n * w)
    return jnp.concatenate(outs, axis=1), ns, rs


def _mixer_tail(z, o_hg, o_gla, p_hgate, p_ggate, p_gate_hg, p_gate_gla, hg_on, gla_on, wbh, wbg, wout):
    on_hg, n_hg, r_hg = _head_norm_fwd(o_hg, hg_on)
    on_gla, n_gla, r_gla = _head_norm_fwd(o_gla, gla_on)
    og_hg = (on_hg * _silu(p_hgate)).astype(BF16)
    og_gla = (on_gla * _silu(p_ggate)).astype(BF16)
    b_hg = jnp.dot(og_hg, wbh, preferred_element_type=F32)
    b_gla = jnp.dot(og_gla, wbg, preferred_element_type=F32)
    s_hg = _sigmoid(p_gate_hg)
    s_gla = _sigmoid(p_gate_gla)
    merged = (s_hg * b_hg + s_gla * b_gla).astype(BF16)
    y1 = jnp.dot(merged, wout, preferred_element_type=F32)
    return dict(on_hg=on_hg, n_hg=n_hg, r_hg=r_hg, on_gla=on_gla, n_gla=n_gla, r_gla=r_gla, og_hg=og_hg,
                og_gla=og_gla, b_hg=b_hg, b_gla=b_gla, s_hg=s_hg, s_gla=s_gla, merged=merged, y1=y1)


def _mixer_tail_fwd(x_lat, p, o_list, modx, norms, onorms, w_br_hg, w_br_gla, w_out, n_ctx_tiles, name):
    rows, d = x_lat.shape

    def body(x_ref, ofw_hg, obw_hg, ofw_gla, obw_gla, p_hgate, p_ggate, p_ghg_a, p_ghg_b, p_ggla_a, p_ggla_b,
             modx_ref, norm_ref, on_ref, wbh_ref, wbg_ref, wout_ref, z2_ref, y1_ref, mrg_ref, oghg_ref, oggla_ref):
        p_gate_hg = jnp.concatenate([p_ghg_a[...], p_ghg_b[...]], axis=1)
        p_gate_gla = jnp.concatenate([p_ggla_a[...], p_ggla_b[...]], axis=1)
        t = _mixer_tail(x_ref[...], ofw_hg[...] + obw_hg[...], ofw_gla[...] + obw_gla[...], p_hgate[...],
                        p_ggate[...], p_gate_hg, p_gate_gla, on_ref[0:1, 0:HD], on_ref[1:2, 0:HD],
                        wbh_ref[...], wbg_ref[...], wout_ref[...])
        y1_ref[...] = t["y1"]
        mrg_ref[...] = t["merged"]
        oghg_ref[...] = t["og_hg"]
        oggla_ref[...] = t["og_gla"]
        n1, _ = _rms(t["y1"])
        z2_ref[...] = x_ref[...] + n1 * norm_ref[1:2, :] * modx_ref[2:3, :]

    lat = lambda i: (i, 0)
    full = lambda i: (i + n_ctx_tiles, 0)
    fixed = lambda i: (0, 0)

    def pcol(blk):
        return pl.BlockSpec((TM, HW), lambda i: (i + n_ctx_tiles, blk))

    in_specs = ([pl.BlockSpec((TM, d), lat)] + [pl.BlockSpec((TM, HW), full)] * 4
                + [pcol(C_HGATE), pcol(C_GGATE), pcol(9), pcol(10), pcol(11), pcol(12)]
                + [pl.BlockSpec((8, d), fixed)] * 3 + [VMEM_SPEC] * 3)
    bf = lambda w: jax.ShapeDtypeStruct((rows, w), BF16)
    f32 = jax.ShapeDtypeStruct((rows, d), F32)
    return pl.pallas_call(
        body, name=name, grid=(rows // TM,), out_shape=[f32, f32, bf(d), bf(HW), bf(HW)], in_specs=in_specs,
        out_specs=[pl.BlockSpec((TM, d), lat)] * 3 + [pl.BlockSpec((TM, HW), lat)] * 2,
        compiler_params=_cparams(dimension_semantics=("parallel",)),
    )(x_lat, *o_list, p, p, p, p, p, p, modx, norms, onorms, w_br_hg, w_br_gla, w_out)


def _ffn_fwd_bwd(z2, modx, norms, w_gate, w_up, w_down, target, name):
    rows, d = z2.shape
    dff = w_gate.shape[0]
    inv_d = 1.0 / d

    def body(z2_ref, modx_ref, norm_ref, wg_ref, wu_ref, wd_ref, t_ref,
             loss_ref, dz2_ref, h2_ref, a_ref, du_ref, dv_ref, dy2_ref, stat_ref):
        i = pl.program_id(0)
        pre2, post2 = norm_ref[2:3, :], norm_ref[3:4, :]
        shift2, scale2, gate2 = modx_ref[3:4, :], modx_ref[4:5, :], modx_ref[5:6, :]
        z2 = z2_ref[...]
        n2, r2 = _rms(z2)
        nw2 = n2 * pre2
        h2 = (nw2 * (1.0 + scale2) + shift2).astype(BF16)
        h2_ref[...] = h2
        u = _dot_nt(h2, wg_ref[...])
        v = _dot_nt(h2, wu_ref[...])
        su = _silu(u)
        a = (su * v).astype(BF16)
        a_ref[...] = a
        y2 = jnp.dot(a, wd_ref[...], preferred_element_type=F32)
        n3, r3 = _rms(y2)
        err = z2 + n3 * post2 * gate2 - t_ref[...]
        part = 0.5 * inv_d * jnp.sum(err * err)
        dz3 = err * inv_d
        dgate2 = _colsum(dz3 * n3 * post2)
        tt = dz3 * gate2
        dpost2 = _colsum(tt * n3)
        dy2 = _rms_bwd(tt * post2, n3, r3).astype(BF16)
        dy2_ref[...] = dy2
        da = _dot_nt(dy2, wd_ref[...])
        du = (da * v * _dsilu(u)).astype(BF16)
        dv = (da * su).astype(BF16)
        du_ref[...] = du
        dv_ref[...] = dv
        dh2 = (jnp.dot(du, wg_ref[...], preferred_element_type=F32)
               + jnp.dot(dv, wu_ref[...], preferred_element_type=F32))
        dshift2 = _colsum(dh2)
        dscale2 = _colsum(dh2 * nw2)
        dnw2 = dh2 * (1.0 + scale2)
        dpre2 = _colsum(dnw2 * n2)
        dz2_ref[...] = dz3 + _rms_bwd(dnw2 * pre2, n2, r2)

        @pl.when(i == 0)
        def _():
            stat_ref[...] = jnp.zeros_like(stat_ref)
            loss_ref[...] = jnp.zeros_like(loss_ref)

        for r, val in enumerate((dshift2, dscale2, dgate2, dpre2, dpost2)):
            stat_ref[r:r + 1, :] += val
        loss_ref[...] += part
        stat_ref[5:6, 0:128] += part

    lat = lambda i: (i, 0)
    fixed = lambda i: (0, 0)
    bf = lambda w: jax.ShapeDtypeStruct((rows, w), BF16)
    return pl.pallas_call(
        body, name=name, grid=(rows // TM,),
        out_shape=[jax.ShapeDtypeStruct((8, 128), F32), jax.ShapeDtypeStruct((rows, d), F32), bf(d), bf(dff), bf(dff),
                   bf(dff), bf(d), jax.ShapeDtypeStruct((8, d), F32)],
        in_specs=[pl.BlockSpec((TM, d), lat), pl.BlockSpec((8, d), fixed), pl.BlockSpec((8, d), fixed)]
        + [VMEM_SPEC] * 3 + [pl.BlockSpec((TM, d), lat)],
        out_specs=[pl.BlockSpec((8, 128), fixed), pl.BlockSpec((TM, d), lat), pl.BlockSpec((TM, d), lat),
                   pl.BlockSpec((TM, dff), lat), pl.BlockSpec((TM, dff), lat), pl.BlockSpec((TM, dff), lat),
                   pl.BlockSpec((TM, d), lat), pl.BlockSpec((8, d), fixed)],
        compiler_params=_cparams(dimension_semantics=("arbitrary",)),
    )(z2, modx, norms, w_gate, w_up, w_down, target)


def _mixer_tail_bwd(x_lat, p, o_list, dz2, y1, modx, norms, onorms, w_br_hg, w_br_gla, w_out, n_ctx_tiles, n_tiles,
                    name):
    rows, d = x_lat.shape
    total = n_tiles * TM

    def body(x_ref, ofw_hg, obw_hg, ofw_gla, obw_gla, p_hgate, p_ggate, p_ghg_a, p_ghg_b, p_ggla_a, p_ggla_b,
             dz2_ref, y1_ref, modx_ref, norm_ref, on_ref, wbh_ref, wbg_ref, wout_ref,
             dohg_ref, dogla_ref, dhgate_ref, dggate_ref, dghg_ref, dggla_ref, dy1_ref, dbhg_ref, dbgla_ref,
             stat_ref):
        i = pl.program_id(0)

        @pl.when(i == 0)
        def _():
            stat_ref[...] = jnp.zeros_like(stat_ref)

        @pl.when(i < n_ctx_tiles)
        def _():
            for ref in (dohg_ref, dogla_ref, dhgate_ref, dggate_ref, dghg_ref, dggla_ref):
                ref[...] = jnp.zeros_like(ref)

        @pl.when(i >= n_ctx_tiles)
        def _():
            post1, gate1 = norm_ref[1:2, :], modx_ref[2:3, :]
            hg_on, gla_on = on_ref[0:1, 0:HD], on_ref[1:2, 0:HD]
            p_gate_hg = jnp.concatenate([p_ghg_a[...], p_ghg_b[...]], axis=1)
            p_gate_gla = jnp.concatenate([p_ggla_a[...], p_ggla_b[...]], axis=1)
            ph, pg = p_hgate[...], p_ggate[...]
            t = _mixer_tail(x_ref[...], ofw_hg[...] + obw_hg[...], ofw_gla[...] + obw_gla[...], ph, pg,
                            p_gate_hg, p_gate_gla, hg_on, gla_on, wbh_ref[...], wbg_ref[...], wout_ref[...])
            dz2 = dz2_ref[...]
            n1, r1 = _rms(y1_ref[...])
            dgate1 = _colsum(dz2 * n1 * post1)
            tt = dz2 * gate1
            dpost1 = _colsum(tt * n1)
            dy1 = _rms_bwd(tt * post1, n1, r1).astype(BF16)
            dy1_ref[...] = dy1
            dmerged = _dot_nt(dy1, wout_ref[...])
            dghg_ref[...] = (dmerged * t["b_hg"] * t["s_hg"] * (1.0 - t["s_hg"])).astype(BF16)
            dggla_ref[...] = (dmerged * t["b_gla"] * t["s_gla"] * (1.0 - t["s_gla"])).astype(BF16)
            db_hg = (dmerged * t["s_hg"]).astype(BF16)
            db_gla = (dmerged * t["s_gla"]).astype(BF16)
            dbhg_ref[...] = db_hg
            dbgla_ref[...] = db_gla
            don_acc = []
            for (db, wb, pgate, on, ns, rs, gain, gate_ref, do_ref) in (
                    (db_hg, wbh_ref, ph, t["on_hg"], t["n_hg"], t["r_hg"], hg_on, dhgate_ref, dohg_ref),
                    (db_gla, wbg_ref, pg, t["on_gla"], t["n_gla"], t["r_gla"], gla_on, dggate_ref, dogla_ref)):
                dog = _dot_nt(db, wb[...])
                gate_ref[...] = (dog * on * _dsilu(pgate)).astype(BF16)
                don = dog * _silu(pgate)
                acc = jnp.zeros((1, HD), F32)
                for h in range(NH):
                    sl = slice(h * HD, (h + 1) * HD)
                    acc = acc + _colsum(don[:, sl] * ns[h])
                    do_ref[:, sl] = _rms_bwd(don[:, sl] * gain, ns[h], rs[h]).astype(BF16)
                don_acc.append(acc)
            stat_ref[0:1, :] += dgate1
            stat_ref[1:2, :] += dpost1
            stat_ref[2:3, 0:HD] += don_acc[0]
            stat_ref[2:3, HD:2 * HD] += don_acc[1]

    lat = lambda i: (jnp.maximum(i - n_ctx_tiles, 0), 0)
    full = lambda i: (i, 0)
    fixed = lambda i: (0, 0)

    def pcol(blk):
        return pl.BlockSpec((TM, HW), lambda i: (i, blk))

    in_specs = ([pl.BlockSpec((TM, d), lat)] + [pl.BlockSpec((TM, HW), full)] * 4
                + [pcol(C_HGATE), pcol(C_GGATE), pcol(9), pcol(10), pcol(11), pcol(12)]
                + [pl.BlockSpec((TM, d), lat), pl.BlockSpec((TM, d), lat)]
                + [pl.BlockSpec((8, d), fixed)] * 3 + [VMEM_SPEC] * 3)
    f = lambda w: jax.ShapeDtypeStruct((total, w), BF16)
    out_shape = [f(HW), f(HW), f(HW), f(HW), f(d), f(d), jax.ShapeDtypeStruct((rows, d), BF16),
                 jax.ShapeDtypeStruct((rows, d), BF16), jax.ShapeDtypeStruct((rows, d), BF16),
                 jax.ShapeDtypeStruct((8, d), F32)]
    out_specs = ([pl.BlockSpec((TM, HW), full)] * 4 + [pl.BlockSpec((TM, d), full)] * 2
                 + [pl.BlockSpec((TM, d), lat)] * 3 + [pl.BlockSpec((8, d), fixed)])
    return pl.pallas_call(
        body, name=name, grid=(n_tiles,), out_shape=out_shape, in_specs=in_specs, out_specs=out_specs,
        compiler_params=_cparams(dimension_semantics=("arbitrary",)),
    )(x_lat, *o_list, p, p, p, p, p, p, dz2, y1, modx, norms, onorms, w_br_hg, w_br_gla, w_out)


def _in_projection_bwd(ctx0, x0, dz2, modc, modx, pre1, w_t, pieces, n_ctx_tiles, name):
    d = x0.shape[1]
    rows = ctx0.shape[0] + x0.shape[0]
    lat_rows = dz2.shape[0]
    width = P_WIDTH
    n_pieces = len(pieces)

    def body(*refs):
        ctx_ref, x_ref, dz2_ref, modc_ref, modx_ref, pre_ref, w_ref = refs[:7]
        (dhq_f, dhq_b, dhi_f, dhi_b, dhf_f, dhf_b, dhgate, dgq_f, dgq_b, dgk_f, dgk_b, dgv_f, dgv_b, dggate,
         dghg, dggla, dlr_f, dlr_b) = refs[7:7 + n_pieces]
        dp_ref, gx_ref, stat_ref = refs[7 + n_pieces:]
        i = pl.program_id(0)
        is_ctx = i < n_ctx_tiles
        z = jnp.where(is_ctx, ctx_ref[...], x_ref[...])
        sections = [
            (0, dhq_f[...] + dhq_b[...]), (HW, dhi_f[...] + dhi_b[...]), (2 * HW, dhf_f[...]), (3 * HW, dhf_b[...]),
            (4 * HW, dhgate[...]), (5 * HW, dgq_f[...] + dgq_b[...]), (6 * HW, dgk_f[...] + dgk_b[...]),
            (7 * HW, dgv_f[...] + dgv_b[...]), (8 * HW, dggate[...]),
            (9 * HW, dghg[:, 0:HW]), (10 * HW, dghg[:, HW:2 * HW]),
            (11 * HW, dggla[:, 0:HW]), (12 * HW, dggla[:, HW:2 * HW]), (OFF_LR, dlr_f[...] + dlr_b[...])]
        dh = jnp.zeros((TM, d), F32)
        for off, val in sections:
            w = val.shape[1]
            vb = val.astype(BF16)
            dp_ref[off:off + w, :] = vb.T
            dh = dh + jnp.dot(vb, w_ref[_w_in_row(off):_w_in_row(off) + w, :], preferred_element_type=F32)
        n, r = _rms(z)
        pre = pre_ref[...]
        scale = jnp.where(is_ctx, modc_ref[1:2, :], modx_ref[1:2, :])
        nw = n * pre
        dshift = _colsum(dh)
        dscale = _colsum(dh * nw)
        dnw = dh * (1.0 + scale)
        dpre = _colsum(dnw * n)
        gx_ref[...] = dz2_ref[...] + _rms_bwd(dnw * pre, n, r)
        zero = jnp.zeros((1, d), F32)

        @pl.when(i == 0)
        def _():
            stat_ref[...] = jnp.zeros_like(stat_ref)

        stat_ref[0:1, :] += jnp.where(is_ctx, zero, dshift)
        stat_ref[1:2, :] += jnp.where(is_ctx, zero, dscale)
        stat_ref[2:3, :] += jnp.where(is_ctx, dshift, zero)
        stat_ref[3:4, :] += jnp.where(is_ctx, dscale, zero)
        stat_ref[4:5, :] += dpre

    full = lambda i: (i, 0)
    lat = lambda i: (jnp.maximum(i - n_ctx_tiles, 0), 0)
    fixed = lambda i: (0, 0)
    piece_specs = [pl.BlockSpec((TM, a.shape[1]), full) for a in pieces]
    in_specs = [pl.BlockSpec((TM, d), lambda i: (jnp.minimum(i, n_ctx_tiles - 1), 0)), pl.BlockSpec((TM, d), lat),
                pl.BlockSpec((TM, d), lat), pl.BlockSpec((8, d), fixed),
                pl.BlockSpec((8, d), fixed), pl.BlockSpec((1, d), fixed), VMEM_SPEC] + piece_specs
    return pl.pallas_call(
        body, name=name, grid=(rows // TM,),
        out_shape=[jax.ShapeDtypeStruct((width, rows), BF16), jax.ShapeDtypeStruct((lat_rows, d), F32),
                   jax.ShapeDtypeStruct((8, d), F32)],
        in_specs=in_specs,
        out_specs=[pl.BlockSpec((width, TM), lambda i: (0, i)), pl.BlockSpec((TM, d), lat),
                   pl.BlockSpec((8, d), fixed)],
        compiler_params=_cparams(dimension_semantics=("arbitrary",)),
    )(ctx0, x0, dz2, modc, modx, pre1, w_t, *pieces)


def _transposed_lhs_matmul(x_ref, dy_ref, o_ref, xt_ref):
    @pl.when(pl.program_id(1) == 0)
    def _():
        xt_ref[...] = x_ref[...].T

    o_ref[...] = jnp.dot(xt_ref[...], dy_ref[...], preferred_element_type=F32)


def _w_in_grad(dp_t, h1, n_cols, name):
    rows, d = h1.shape
    n_main = OFF_LR // HW
    lr0 = _w_in_row(OFF_LR)

    def body(x_ref, xlr_ref, h_ref, o_hbm, acc_ref, sem):
        i = pl.program_id(0)

        def main_copy(step):
            row = jnp.where(step < 9, step * HW, step * HW + 2 * RANK)
            return pltpu.make_async_copy(acc_ref, o_hbm.at[pl.ds(pl.multiple_of(row, 8), HW), :], sem)

        lr_copy = pltpu.make_async_copy(acc_ref.at[0:2 * RANK, :], o_hbm.at[lr0:lr0 + 2 * RANK, :], sem)

        @pl.when(i > 0)
        def _():
            main_copy(i - 1).wait()

        @pl.when(i < n_main)
        def _():
            acc_ref[...] = jnp.dot(x_ref[...], h_ref[...], preferred_element_type=F32)
            main_copy(i).start()

        @pl.when(i == n_main)
        def _():
            acc_ref[0:128, :] = jnp.dot(xlr_ref[...], h_ref[...], preferred_element_type=F32)
            lr_copy.start()
            lr_copy.wait()

    return pl.pallas_call(
        body, name=name, grid=(n_main + 1,),
        out_shape=jax.ShapeDtypeStruct((n_cols, d), F32),
        in_specs=[pl.BlockSpec((HW, rows), lambda i: (jnp.minimum(i, n_main - 1), 0)),
                  pl.BlockSpec((128, rows), lambda i: (OFF_LR // 128, 0)),
                  pl.BlockSpec((rows, d), lambda i: (0, 0))],
        out_specs=ANY_SPEC,
        scratch_shapes=[pltpu.VMEM((HW, d), F32), pltpu.SemaphoreType.DMA],
        compiler_params=_cparams(dimension_semantics=("arbitrary",)),
    )(dp_t, dp_t, h1)


def _weight_grad(xs, dy, name, tk=None, tn=512, k_first=0, k_tiles=None):
    rows = dy.shape[0]
    n = dy.shape[1]
    tn_ = min(tn, n)
    tk_ = xs.shape[1] if tk is None else tk
    k_tiles = xs.shape[1] // tk_ if k_tiles is None else k_tiles
    k = k_tiles * tk_

    return pl.pallas_call(
        functools.partial(_transposed_lhs_matmul), name=name, grid=(k_tiles, n // tn_),
        out_shape=jax.ShapeDtypeStruct((k, n), F32),
        in_specs=[pl.BlockSpec((rows, tk_), lambda i, j: (0, i + k_first)),
                  pl.BlockSpec((rows, tn_), lambda i, j: (0, j))],
        out_specs=pl.BlockSpec((tk_, tn_), lambda i, j: (i, j)),
        scratch_shapes=[pltpu.VMEM((tk_, rows), BF16)],
        compiler_params=_cparams(dimension_semantics=("parallel", "arbitrary")),
    )(xs, dy)


def _running_sums(xs, fws):
    c = xs[0].shape[0]
    row = lax.broadcasted_iota(jnp.int32, (c, 1), 0)
    s = 1
    while s < c:
        xs = [x + (jnp.where(row >= s, pltpu.roll(x, s, axis=0), 0.0) if fw else
                   jnp.where(row < c - s, pltpu.roll(x, c - s, axis=0), 0.0)) for x, fw in zip(xs, fws)]
        s *= 2
    return xs


def _chunks_terms(qs, ks, gs, fws):
    c = CHUNK
    n = len(qs)
    r = lax.broadcasted_iota(jnp.int32, (c, c), 0)
    s = lax.broadcasted_iota(jnp.int32, (c, c), 1)
    row = lax.broadcasted_iota(jnp.int32, (c, 1), 0)
    per_dir = {}
    for fw in set(fws):
        pos = row if fw else (c - 1 - row)
        per_dir[fw] = dict(
            causal=(s <= r) if fw else (s >= r), causal_t=(s >= r) if fw else (s <= r), pos=pos,
            in_blk=[(pos >= SUB * j) & (pos < SUB * (j + 1)) for j in range(NSUB)],
            start_row=[None] + [SUB * j - 1 if fw else c - SUB * j for j in range(1, NSUB)],
            rend=c - 1 if fw else 0)
    dirs = [per_dir[fw] for fw in fws]
    cums = _running_sums(gs, fws)
    starts = [[None] + [cum[d["start_row"][j]:d["start_row"][j] + 1, :] for j in range(1, NSUB)]
              for cum, d in zip(cums, dirs)]
    es = [[jnp.exp(cum) for cum in cums]]
    for j in range(1, NSUB):
        es.append([jnp.exp(jnp.where(d["pos"] >= SUB * j, cum - st[j], -1e30)) for cum, st, d in zip(cums, starts, dirs)])
    owns = [functools.reduce(lambda rest, j: jnp.where(d["in_blk"][j], st[j], rest), range(1, NSUB), 0.0)
            for st, d in zip(starts, dirs)]
    kscales = [jnp.exp(own - cum) for own, cum in zip(owns, cums)]
    cends = [cum[d["rend"]:d["rend"] + 1, :] for cum, d in zip(cums, dirs)]
    tails = [jnp.exp(cend - cum) for cend, cum in zip(cends, cums)]
    qcats = [jnp.concatenate([q * es[j][i] for j in range(NSUB)], axis=1).astype(BF16) for i, q in enumerate(qs)]
    kts = [k * ksc for k, ksc in zip(ks, kscales)]
    kms = [jnp.concatenate([jnp.where(d["in_blk"][j], kt, 0.0) for j in range(NSUB)], axis=1).astype(BF16)
           for kt, d in zip(kts, dirs)]
    e_by_lane = [[es[j][i] for j in range(NSUB)] for i in range(n)]
    return dict(dirs=dirs, e=e_by_lane, kscale=kscales, cend=cends, tail=tails, qcat=qcats, km=kms, kt=kts)


def _chunks_fwd(qs, ks, vs, gs, st0s, fws):
    t = _chunks_terms(qs, ks, gs, fws)
    scores = [_dot_nt(qc, km) for qc, km in zip(t["qcat"], t["km"])]
    a = [jnp.where(d["causal"], sc, 0.0) for sc, d in zip(scores, t["dirs"])]
    inter = [_dot_nt(qc[:, 0:HD], st0) for qc, st0 in zip(t["qcat"], st0s)]
    intra = [_dot(a_, v) for a_, v in zip(a, vs)]
    os_ = [x + y for x, y in zip(intra, inter)]
    upd = [_dot_tn(v, k * tl) for v, k, tl in zip(vs, ks, t["tail"])]
    st1s = [st0 * jnp.exp(ce) + u for st0, ce, u in zip(st0s, t["cend"], upd)]
    return os_, st1s


def _chunks_bwd(qs, ks, vs, gs, st0s, dos, dst1s, fws):
    n = len(qs)
    t = _chunks_terms(qs, ks, gs, fws)
    qcat, km, e, dirs = t["qcat"], t["km"], t["e"], t["dirs"]
    a_t = [jnp.where(d["causal_t"], _dot_nt(km_, qc), 0.0) for km_, qc, d in zip(km, qcat, dirs)]
    ktail = [k * tl for k, tl in zip(ks, t["tail"])]
    dv_a = [_dot(at, do) for at, do in zip(a_t, dos)]
    dv_b = [_dot_nt(kt, ds) for kt, ds in zip(ktail, dst1s)]
    dv = [x + y for x, y in zip(dv_a, dv_b)]
    da = [jnp.where(d["causal"], _dot_nt(do, v), 0.0) for do, v, d in zip(dos, vs, dirs)]
    da_t = [jnp.where(d["causal_t"], _dot_nt(v, do), 0.0) for do, v, d in zip(dos, vs, dirs)]
    dqcat = [_dot(da_, km_) for da_, km_ in zip(da, km)]
    dq_inter = [e[i][0] * _dot(dos[i], st0s[i]) for i in range(n)]
    dkm = [_dot(dat, qc) for dat, qc in zip(da_t, qcat)]
    dk_inter = [_dot(v, ds) * tl for v, ds, tl in zip(vs, dst1s, t["tail"])]
    dq = [dq_inter[i] + sum(e[i][j] * dqcat[i][:, j * HD:(j + 1) * HD] for j in range(NSUB)) for i in range(n)]
    dkt = [sum(jnp.where(dirs[i]["in_blk"][j], dkm[i][:, j * HD:(j + 1) * HD], 0.0) for j in range(NSUB))
           for i in range(n)]
    dk = [dkt[i] * t["kscale"][i] + dk_inter[i] for i in range(n)]
    dcum = [qs[i] * dq_inter[i] - ks[i] * dk_inter[i] - t["kt"][i].astype(BF16).astype(F32) * dkt[i]
            + sum(qcat[i][:, j * HD:(j + 1) * HD].astype(F32) * dqcat[i][:, j * HD:(j + 1) * HD] for j in range(NSUB))
            for i in range(n)]
    ecend = [jnp.exp(ce) for ce in t["cend"]]
    end = [ecend[i] * _colsum(st0s[i] * dst1s[i]) + _colsum(ks[i] * dk_inter[i]) for i in range(n)]
    sums = _running_sums(dcum, [not fw for fw in fws])
    dg = [sm + en for sm, en in zip(sums, end)]
    upd = [_dot_tn(dos[i], qs[i] * e[i][0]) for i in range(n)]
    dst0 = [dst1s[i] * ecend[i] + upd[i] for i in range(n)]
    return dq, dk, dv, dg, dst0


def _chunk_index(step, n_ctx_chunks, n_chunks, fw):
    if fw:
        return step
    return jnp.where(step < n_ctx_chunks, n_ctx_chunks - 1 - step, n_chunks - 1 + n_ctx_chunks - step)


def _hg_inputs(hq, hf, lbv, d_idx, sl):
    lb = _sigmoid(lbv[d_idx:d_idx + 1, sl] - lbv[2 + d_idx:3 + d_idx, sl])
    sg = _sigmoid(hf)
    f = lb + (1.0 - lb) * sg
    return _silu(hq), 1.0 - f, jnp.log(f), f, sg, lb


def _scan_fwd_both(p, branch_sides, n_ctx_chunks, name):
    rows = p.shape[0]
    n_chunks = rows // CHUNK
    n_ins = [4 if branch == "hg" else 6 for branch, _ in branch_sides]
    n_in_all = 2 * sum(n_ins)
    n_br = len(branch_sides)

    def body(*refs):
        ins, outs, state = refs[:n_in_all], refs[n_in_all:n_in_all + 4 * n_br], refs[-1]

        @pl.when(pl.program_id(0) == 0)
        def _():
            state[...] = jnp.zeros_like(state)

        lanes, where = [], []
        pos = 0
        for bi, (branch, _) in enumerate(branch_sides):
            hg = branch == "hg"
            n_in = n_ins[bi]
            for di, fw in enumerate((True, False)):
                r = ins[pos:pos + n_in]
                pos += n_in
                o_ref, st_ref = outs[4 * bi + 2 * di], outs[4 * bi + 2 * di + 1]
                if hg:
                    a_ref, b_ref, c_ref, lb_ref = r
                else:
                    a_ref, b_ref, c_ref, lr_ref, wgk_ref, bgk_ref = r
                    logits = _dot(lr_ref[...], wgk_ref[...]) + bgk_ref[...]
                    g_all = _log_sigmoid(logits) * (1.0 / GATE_NORM)
                for h in range(NH):
                    sl = slice(h * HD, (h + 1) * HD)
                    if hg:
                        q, k, g, _, _, _ = _hg_inputs(a_ref[:, sl], c_ref[:, sl], lb_ref[...], di, sl)
                        v = b_ref[:, sl]
                    else:
                        q, k, v, g = a_ref[:, sl] * (HD ** -0.5), b_ref[:, sl], c_ref[:, sl], g_all[:, sl]
                    lanes.append((q, k, v, g, state[2 * bi + di, h], fw))
                    where.append((2 * bi + di, h, sl, o_ref, st_ref))
        qs, ks, vs, gs, st0s, fws = (list(col) for col in zip(*lanes))
        os_, st1s = _chunks_fwd(qs, ks, vs, gs, st0s, fws)
        for (si, h, sl, o_ref, st_ref), st0, o, st1 in zip(where, st0s, os_, st1s):
            st_ref[0, h] = st0
            o_ref[:, sl] = o
            state[si, h] = st1

    fixed = lambda j: (0, 0)
    in_specs, args, out_specs = [], [], []
    for branch, side in branch_sides:
        for di, fw in enumerate((True, False)):
            chunk = functools.partial(_chunk_index, n_ctx_chunks=n_ctx_chunks, n_chunks=n_chunks, fw=fw)

            def cmap(blk, width=HW, chunk=chunk):
                return pl.BlockSpec((CHUNK, width), lambda j: (chunk(j), blk))

            if branch == "hg":
                in_specs += [cmap(C_HQ), cmap(C_HI), cmap(C_HF_FW + di), pl.BlockSpec((4, HW), fixed)]
                args += [p, p, p, side]
            else:
                in_specs += [cmap(C_GQ), cmap(C_GK), cmap(C_GV), cmap(OFF_LR // 128, 128),
                             pl.BlockSpec((128, HW), fixed), pl.BlockSpec((1, HW), fixed)]
                args += [p, p, p, p, side[di][0], side[di][1]]
            out_specs += [cmap(0), pl.BlockSpec((1, NH, HD, HD), lambda j, chunk=chunk: (chunk(j), 0, 0, 0))]
    return pl.pallas_call(
        body, name=name, grid=(n_chunks,),
        out_shape=[jax.ShapeDtypeStruct((rows, HW), F32),
                   jax.ShapeDtypeStruct((n_chunks, NH, HD, HD), F32)] * (2 * n_br),
        in_specs=in_specs, out_specs=out_specs,
        scratch_shapes=[pltpu.VMEM((2 * n_br, NH, HD, HD), F32)],
        compiler_params=_cparams(dimension_semantics=("arbitrary",)),
    )(*args)


def _scan_bwd_both(p, branch_items, n_ctx_chunks, name):
    rows = p.shape[0]
    n_chunks = rows // CHUNK
    n_ins = [6 if item[0] == "hg" else 8 for item in branch_items]
    n_outs = [4 if item[0] == "hg" else 6 for item in branch_items]
    n_in_all, n_out_all = 2 * sum(n_ins), 2 * sum(n_outs)

    def body(*refs):
        ins, outs, dstate = refs[:n_in_all], refs[n_in_all:n_in_all + n_out_all], refs[-1]
        first = pl.program_id(0) == 0

        @pl.when(first)
        def _():
            dstate[...] = jnp.zeros_like(dstate)

        lanes, where, extra, ctx = [], [], [], []
        ipos = opos = 0
        for bi, item in enumerate(branch_items):
            hg = item[0] == "hg"
            for di, fw in enumerate((True, False)):
                r, w = ins[ipos:ipos + n_ins[bi]], outs[opos:opos + n_outs[bi]]
                ipos += n_ins[bi]
                opos += n_outs[bi]
                if hg:
                    a_ref, b_ref, c_ref, lb_ref, st_ref, do_ref = r
                    acc_refs = (w[3],)
                else:
                    a_ref, b_ref, c_ref, lr_ref, wgk_ref, bgk_ref, st_ref, do_ref = r
                    acc_refs = (w[4], w[5])
                    lr = lr_ref[...]
                    logits = _dot(lr, wgk_ref[...]) + bgk_ref[...]
                    g_all = _log_sigmoid(logits) * (1.0 / GATE_NORM)

                @pl.when(first)
                def _(acc_refs=acc_refs):
                    for ref in acc_refs:
                        ref[...] = jnp.zeros_like(ref)

                for h in range(NH):
                    sl = slice(h * HD, (h + 1) * HD)
                    if hg:
                        hq, hf = a_ref[:, sl], c_ref[:, sl]
                        q, k, g, f, sg, lb = _hg_inputs(hq, hf, lb_ref[...], di, sl)
                        v = b_ref[:, sl]
                        extra.append((hq, f, sg, lb))
                    else:
                        q, k, v, g = a_ref[:, sl] * (HD ** -0.5), b_ref[:, sl], c_ref[:, sl], g_all[:, sl]
                        extra.append(None)
                    lanes.append((q, k, v, g, st_ref[0, h], do_ref[:, sl], dstate[2 * bi + di, h], fw))
                    where.append((2 * bi + di, h, sl))
                ctx.append((hg, w, None if hg else (lr, logits, wgk_ref)))

        dqs, dks, dvs, dgs, dst0s = [], [], [], [], []
        for lo in range(0, len(lanes), BWD_LANES):
            cols = [list(col) for col in zip(*lanes[lo:lo + BWD_LANES])]
            for acc, part in zip((dqs, dks, dvs, dgs, dst0s), _chunks_bwd(*cols)):
                acc.extend(part)
        dg_parts = [[] for _ in ctx]
        for (si, h, sl), ex, dq, dk, dv, dg, dst0 in zip(where, extra, dqs, dks, dvs, dgs, dst0s):
            dstate[si, h] = dst0
            hg, w, _ = ctx[si]
            if hg:
                hq, f, sg, lb = ex
                da_ref, db_ref, dc_ref, dlb_ref = w
                da_ref[:, sl] = (dq * _dsilu(hq)).astype(BF16)
                db_ref[:, sl] = dv.astype(BF16)
                df = dg / f - dk
                dc_ref[:, sl] = (df * (1.0 - lb) * sg * (1.0 - sg)).astype(BF16)
                dlb_ref[0:1, sl] += _colsum(df * (1.0 - sg))
            else:
                da_ref, db_ref, dc_ref = w[:3]
                da_ref[:, sl] = (dq * (HD ** -0.5)).astype(BF16)
                db_ref[:, sl] = dk.astype(BF16)
                dc_ref[:, sl] = dv.astype(BF16)
                dg_parts[si].append(dg)
        for si, (hg, w, more) in enumerate(ctx):
            if not hg:
                dlr_ref, dwgk_ref, dbias_ref = w[3:]
                lr, logits, wgk_ref = more
                dlogits = jnp.concatenate(dg_parts[si], axis=1) * (1.0 / GATE_NORM) * (1.0 - _sigmoid(logits))
                dlr_ref[...] = _dot_nt(dlogits, wgk_ref[...]).astype(BF16)
                dwgk_ref[...] += _dot_tn(lr, dlogits)
                dbias_ref[0:1, :] += _colsum(dlogits)

    fixed = lambda j: (0, 0)
    big = jax.ShapeDtypeStruct((rows, HW), BF16)
    in_specs, args, out_shape, out_specs = [], [], [], []
    for branch, side, states, d_o in branch_items:
        for di, fw in enumerate((True, False)):
            def chunk_of(j, fw=fw):
                return _chunk_index(n_chunks - 1 - j, n_ctx_chunks, n_chunks, fw)

            def cmap(blk, width=HW, chunk_of=chunk_of):
                return pl.BlockSpec((CHUNK, width), lambda j: (chunk_of(j), blk))

            st_spec = pl.BlockSpec((1, NH, HD, HD), lambda j, chunk_of=chunk_of: (chunk_of(j), 0, 0, 0))
            if branch == "hg":
                in_specs += [cmap(C_HQ), cmap(C_HI), cmap(C_HF_FW + di), pl.BlockSpec((4, HW), fixed), st_spec,
                             cmap(0)]
                args += [p, p, p, side, states[di], d_o]
                out_shape += [big, big, big, jax.ShapeDtypeStruct((8, HW), F32)]
                out_specs += [cmap(0), cmap(0), cmap(0), pl.BlockSpec((8, HW), fixed)]
            else:
                in_specs += [cmap(C_GQ), cmap(C_GK), cmap(C_GV), cmap(OFF_LR // 128, 128),
                             pl.BlockSpec((128, HW), fixed), pl.BlockSpec((1, HW), fixed), st_spec, cmap(0)]
                args += [p, p, p, p, side[di][0], side[di][1], states[di], d_o]
                out_shape += [big, big, big, jax.ShapeDtypeStruct((rows, 128), BF16),
                              jax.ShapeDtypeStruct((128, HW), F32), jax.ShapeDtypeStruct((8, HW), F32)]
                out_specs += [cmap(0), cmap(0), cmap(0), cmap(0, 128), pl.BlockSpec((128, HW), fixed),
                              pl.BlockSpec((8, HW), fixed)]
    return pl.pallas_call(
        body, name=name, grid=(n_chunks,), out_shape=out_shape, in_specs=in_specs, out_specs=out_specs,
        scratch_shapes=[pltpu.VMEM((2 * len(branch_items), NH, HD, HD), F32)],
        compiler_params=_cparams(dimension_semantics=("arbitrary",)),
    )(*args)


SMALL_ROWS = 56
ROWS_MOD_X = (0, 1, 8, 16, 17, 18)
ROWS_MOD_C = (2, 3)
ROW_PRE1, ROW_POST1, ROW_ONORM, ROW_PRE2, ROW_POST2, ROW_LB, ROW_BGK, ROW_WGK = 4, 9, 10, 19, 20, 24, 32, 40
ROW_LOSS = 21


def _reduce_small(gathered, lb_full, name):
    _, _, d = gathered.shape

    def body(g_ref, lb_ref, sum_ref, dmod_ref, dbmod_ref, dlb_ref):
        total = g_ref[0]
        for b in range(1, N_DEV):
            total = total + g_ref[b]
        sum_ref[...] = total
        dmod_ref[...] = jnp.zeros_like(dmod_ref)
        for m in range(N_MOD):
            col = slice(m * d, (m + 1) * d)
            acc = jnp.zeros((1, d), F32)
            for b in range(N_DEV):
                row = g_ref[b, ROWS_MOD_X[m]:ROWS_MOD_X[m] + 1, :]
                dmod_ref[b:b + 1, col] = row
                acc = acc + row
            if m < 2:
                ctx_row = total[ROWS_MOD_C[m]:ROWS_MOD_C[m] + 1, :]
                dmod_ref[8:9, col] = ctx_row
                acc = acc + ctx_row
            dbmod_ref[:, col] = acc
        lbv = lb_ref[...]
        for dd in range(2):
            lb = _sigmoid(lbv[dd:dd + 1, :] - lbv[2 + dd:3 + dd, :])
            gl = total[ROW_LB:ROW_LB + 1, dd * HW:(dd + 1) * HW] * lb * (1.0 - lb)
            dlb_ref[dd:dd + 1, :] = gl
            dlb_ref[2 + dd:3 + dd, :] = -gl

    return pl.pallas_call(
        body, name=name,
        out_shape=[jax.ShapeDtypeStruct((SMALL_ROWS, d), F32), jax.ShapeDtypeStruct((16, N_MOD * d), F32),
                   jax.ShapeDtypeStruct((1, N_MOD * d), F32), jax.ShapeDtypeStruct((4, HW), F32)],
        in_specs=[VMEM_SPEC] * 2, out_specs=[VMEM_SPEC] * 4, compiler_params=_cparams(),
    )(gathered, lb_full)


def _c_ctx_grad(gathered, c_ctx_row, name):
    def body(g_ref, c_ref, o_ref):
        acc = g_ref[0, 0:1, :]
        for chip in range(1, N_CHIP):
            acc = acc + g_ref[2 * chip, 0:1, :]
        o_ref[...] = acc * _dsilu(c_ref[...])

    return pl.pallas_call(
        body, name=name, out_shape=jax.ShapeDtypeStruct(c_ctx_row.shape, F32),
        in_specs=[VMEM_SPEC] * 2, out_specs=VMEM_SPEC, compiler_params=_cparams(),
    )(gathered, c_ctx_row)


def _blocked(full, n_blocks):
    k, n = full.shape
    return full.reshape(k, n_blocks, n // n_blocks).transpose(1, 0, 2)


def _unblocked(blocks):
    nb, k, n = blocks.shape
    return blocks.transpose(1, 0, 2).reshape(k, nb * n)


def _sample_front(x0, ctx0, modc, modx, norm_pre1, lb_full, gla_side, w_in_r):
    ctx_len = ctx0.shape[0]
    n_ctx_tiles = ctx_len // TM
    n_ctx_chunks = ctx_len // CHUNK
    h1, p = _in_projection(ctx0, x0, modc, modx, norm_pre1, w_in_r, n_ctx_tiles, "in_projection")
    (o_hg_fw, st_hg_fw, o_hg_bw, st_hg_bw, o_gla_fw, st_gla_fw, o_gla_bw, st_gla_bw) = _scan_fwd_both(
        p, [("hg", lb_full), ("gla", gla_side)], n_ctx_chunks, "scan_fwd")
    return dict(h1=h1, p=p, o_list=[o_hg_fw, o_hg_bw, o_gla_fw, o_gla_bw],
                states=[st_hg_fw, st_hg_bw, st_gla_fw, st_gla_bw])


def _sample_back(reduce, front, x0, ctx0, target0, modc, modx, norm_pre1, norms, onorms, lb_full, gla_side, w_in_r,
                 wbh, wbg, wout, ffn_weights):
    seq, d = x0.shape
    ctx_len = ctx0.shape[0]
    n_ctx_tiles = ctx_len // TM
    n_tiles = (ctx_len + seq) // TM
    n_ctx_chunks = ctx_len // CHUNK
    h1, p, o_list = front["h1"], front["p"], front["o_list"]
    st_hg_fw, st_hg_bw, st_gla_fw, st_gla_bw = front["states"]
    z2, y1, merged, og_hg, og_gla = _mixer_tail_fwd(x0, p, o_list, modx, norms, onorms, wbh, wbg, wout, n_ctx_tiles,
                                                    "mixer_tail")
    wg, wu, wd = ffn_weights([z2])
    loss_part, dz2, h2, a_act, du, dv, dy2, stat_ffn = _ffn_fwd_bwd(z2, modx, norms, wg, wu, wd, target0, "ffn")
    dff = wg.shape[0]
    tok = reduce("ffn", [_weight_grad(du, h2, "grad_w_ff_gate", tk=dff // 2, tn=d),
                         _weight_grad(dv, h2, "grad_w_ff_up", tk=dff // 2, tn=d),
                         _weight_grad(a_act, dy2, "grad_w_ff_down", tk=dff // 2)])

    (d_ohg, d_ogla, d_hgate, d_ggate, d_ghg, d_ggla, dy1, db_hg, db_gla, stat_mix) = _mixer_tail_bwd(
        x0, p, o_list, dz2, y1, modx + tok, norms, onorms, wbh, wbg, wout, n_ctx_tiles, n_tiles, "mixer_tail_bwd")
    tok = reduce("mix", [_weight_grad(og_hg, db_hg, "grad_w_br_hg"), _weight_grad(og_gla, db_gla, "grad_w_br_gla"),
                         _weight_grad(merged, dy1, "grad_w_out")])
    tok = tok + reduce("push_ffn", [dy1])
    gla_b = [(wgk, bias + tok) for wgk, bias in gla_side]
    (dgq_f, dgk_f, dgv_f, dlr_f, dwgk_f, dbgk_f, dgq_b, dgk_b, dgv_b, dlr_b, dwgk_b, dbgk_b,
     dhq_f, dhi_f, dhf_f, dlb_f, dhq_b, dhi_b, dhf_b, dlb_b) = _scan_bwd_both(
        p, [("gla", gla_b, (st_gla_fw, st_gla_bw), d_ogla), ("hg", lb_full, (st_hg_fw, st_hg_bw), d_ohg)],
        n_ctx_chunks, "scan_bwd")
    tok = reduce("push_mix", [dbgk_f])
    pieces = [dhq_f, dhq_b, dhi_f, dhi_b, dhf_f, dhf_b, d_hgate, dgq_f, dgq_b, dgk_f, dgk_b, dgv_f, dgv_b, d_ggate,
              d_ghg, d_ggla, dlr_f, dlr_b]
    dp, grad_x, stat_in = _in_projection_bwd(ctx0, x0, dz2, modc, modx, norm_pre1 + tok, w_in_r, pieces, n_ctx_tiles,
                                             "in_projection_bwd")

    tok = reduce("in", [_w_in_grad(dp, h1, w_in_r.shape[0], "grad_w_in")])
    reduce("small", dict(stat_in=stat_in + tok, stat_mix=stat_mix, stat_ffn=stat_ffn, dlb=(dlb_f, dlb_b),
                         dwgk=(dwgk_f, dwgk_b), dbgk=(dbgk_f, dbgk_b)))
    reduce("push_in", [])
    return dict(loss_part=loss_part, grad_x=grad_x)


def kernel(x, c, ctx, c_ctx, w_mod, b_mod, norm_pre1, norm_post1, norm_pre2, norm_post2, w_in, hg_lb, hg_onorm, gla_w_gk, gla_b_gk, gla_onorm, w_br_hg, w_br_gla, w_out, w_ff_gate, w_ff_up, w_ff_down, loss_target, m_c_ctx, m_w_mod, m_b_mod, m_norm_pre1, m_norm_post1, m_norm_pre2, m_norm_post2, m_w_in, m_hg_lb, m_hg_onorm, m_gla_w_gk, m_gla_b_gk, m_gla_onorm, m_w_br_hg, m_w_br_gla, m_w_out, m_w_ff_gate, m_w_ff_up, m_w_ff_down, v_c_ctx, v_w_mod, v_b_mod, v_norm_pre1, v_norm_post1, v_norm_pre2, v_norm_post2, v_w_in, v_hg_lb, v_hg_onorm, v_gla_w_gk, v_gla_b_gk, v_gla_onorm, v_w_br_hg, v_w_br_gla, v_w_out, v_w_ff_gate, v_w_ff_up, v_w_ff_down):
    seq, d = x.shape[1], x.shape[2]
    ctx_len = ctx.shape[1]
    assert seq % TM == 0 and ctx_len % TM == 0 and d == 2 * HW
    ax, ay, ac = lax.axis_index("x"), lax.axis_index("y"), lax.axis_index("c")
    chip = 2 * ax + ay
    dev = 2 * chip + ac
    c_arr = jnp.reshape(ac, (1,)).astype(jnp.int32)
    chip_arr = jnp.reshape(chip, (1,)).astype(jnp.int32)
    transposed = ("w_in", "w_ff_gate", "w_ff_up")
    view = lambda a, nm: a[0].T if nm in transposed else a[0]

    sems_in, lands_in, token_in0 = _blocks_start([_cast_into_blocks(chip_arr, view(w_in, "w_in"), "cast_w_in")],
                                                 "gather_w_in_start")

    nc = d // 128
    pad8 = lambda a: jnp.pad(a, ((0, -a.shape[0] % 8), (0, 0)))
    small1 = jnp.concatenate([c.reshape(nc, 128) + token_in0[0, 0], pad8(hg_lb.reshape(4, 128)),
                              gla_w_gk.reshape(2 * RANK, 128), pad8(gla_b_gk.reshape(2, 128))], axis=0)
    blocks = [_cast_into_blocks(chip_arr, view(w_, nm), "cast_" + nm) for w_, nm in (
        (w_br_hg, "w_br_hg"), (w_br_gla, "w_br_gla"), (w_out, "w_out"), (w_ff_gate, "w_ff_gate"),
        (w_ff_up, "w_ff_up"), (w_ff_down, "w_ff_down"))]
    got1 = _allgather8(small1, "gather_small_params", after=blocks)
    c_all = got1[:, :nc, :].reshape(N_DEV, d)
    per_chip = got1[0::2]
    lb_full = per_chip[:, nc:nc + 4, :].transpose(1, 0, 2).reshape(4, HW)
    wgk_full = per_chip[:, nc + 8:nc + 8 + 2 * RANK, :].transpose(1, 0, 2).reshape(2, RANK, HW)
    bgk_full = per_chip[:, nc + 8 + 2 * RANK:nc + 10 + 2 * RANK, :].transpose(1, 0, 2).reshape(2, HW)
    wgk_pad = [jnp.zeros((128, HW), F32).at[dd * RANK:(dd + 1) * RANK].set(wgk_full[dd]) for dd in range(2)]
    bgk = [bgk_full[dd:dd + 1] for dd in range(2)]

    n_mod_cols = w_mod.shape[2]
    cond = jnp.concatenate([c_all, pad8(c_ctx.reshape(1, d))], axis=0)
    b_cols = lax.dynamic_slice(b_mod, (0, chip * n_mod_cols), (1, n_mod_cols))
    lands_in = _blocks_wait(sems_in, lands_in, [got1], "gather_w_in_wait")
    fwd_sems, lands_in, fwd_token = _forward_start(lands_in, "gather_w_in_forward_start")
    mod_part = _mod_forward(cond + fwd_token[0, 0], w_mod[0], b_cols, "mod_forward")
    mod_got = _allgather8(mod_part, "gather_mod")
    mod_all = mod_got[0::2].transpose(1, 0, 2).reshape(16, N_CHIP * n_mod_cols)
    modx = pad8(lax.dynamic_slice(mod_all, (dev, 0), (1, N_MOD * d)).reshape(N_MOD, d))
    modc = pad8(mod_all[8].reshape(N_MOD, d))

    gathered_in = _forward_wait(fwd_sems, lands_in, [mod_got], "gather_w_in_forward_wait")
    sems, lands, token = _blocks_start(blocks, "gather_rest_start", after=[gathered_in[0]])
    w_in_r = gathered_in[0].reshape(-1, d)

    norms = jnp.concatenate([norm_pre1, norm_post1, norm_pre2, norm_post2, jnp.zeros((4, d), F32)], axis=0)
    onorms = jnp.zeros((8, d), F32).at[0, :HD].set(hg_onorm[0]).at[1, :HD].set(gla_onorm[0])
    gla_side = [(wgk_pad[dd], bgk[dd]) for dd in range(2)]
    modx = modx + token[0, 0]
    front = _sample_front(x[0], ctx[0], modc, modx, norm_pre1, lb_full, gla_side, w_in_r)
    lands = _blocks_wait(sems, lands, front["o_list"], "gather_rest_wait")
    gathered = _blocks_finish(lands[:3], "gather_mix_finish")
    wbh, wbg = _unblocked(gathered[0]), _unblocked(gathered[1])
    wout = gathered[2].reshape(d, d)
    ffn_sems, ffn_lands, ffn_token = _forward_start(lands[3:], "gather_ffn_forward_start")
    onorms = onorms + ffn_token[0, 0]

    def ffn_weights(after):
        got = _forward_wait(ffn_sems, ffn_lands, after, "gather_ffn_forward_wait")
        return tuple(g.reshape(-1, d) for g in got)

    dff = w_ff_down.shape[1] * N_CHIP
    groups = {"ffn": ["w_ff_gate", "w_ff_up", "w_ff_down"], "mix": ["w_br_hg", "w_br_gla", "w_out"], "in": ["w_in"]}
    row_sharded = {"w_out": d // N_CHIP, "w_ff_down": dff // N_CHIP, "w_ff_gate": dff // N_CHIP,
                   "w_ff_up": dff // N_CHIP, "w_in": w_in.shape[2]}
    in_flight, to_sibling, small = {}, {}, {}

    def reduce_small(stats):
        small2 = jnp.concatenate([
            stats["stat_in"], stats["stat_mix"], stats["stat_ffn"],
            jnp.concatenate(stats["dlb"], axis=1), jnp.concatenate(stats["dbgk"], axis=1),
            jnp.concatenate([stats["dwgk"][0][0:RANK], stats["dwgk"][1][RANK:2 * RANK]], axis=1)], axis=0)
        assert small2.shape[0] == SMALL_ROWS
        got2 = _allgather8(small2, "gather_small_grads")
        total, dmod_all, g_b_mod, g_lb_full = _reduce_small(got2, lb_full, "reduce_small")
        dmod_cols = lax.dynamic_slice(dmod_all, (0, chip * n_mod_cols), (16, n_mod_cols))
        g_w_mod, cctx_part = _mod_backward(cond, w_mod[0], dmod_cols, "mod_backward")
        got3 = _allgather8(cctx_part, "gather_c_ctx_grad")
        g_c_ctx = _c_ctx_grad(got3, c_ctx.reshape(1, d), "c_ctx_grad")
        small.update(total=total, g_b_mod=g_b_mod, g_lb_full=g_lb_full, g_w_mod=g_w_mod, g_c_ctx=g_c_ctx)

    def reduce(group, grads):
        if group == "small":
            return reduce_small(grads)
        if group.startswith("push_"):
            return push(group[5:], grads)
        nms = groups[group]
        full = [g.reshape(N_CHIP, row_sharded[nm], d) if nm in row_sharded else _blocked(g, N_CHIP)
                for g, nm in zip(grads, nms)]
        sems_, full, lands_, token_ = _send_half_start(full, "grads_to_sibling_start_" + group)
        to_sibling[group] = (sems_, full, lands_)
        return token_[0, 0]

    def push(group, after):
        nms = groups[group]
        sems_, full, lands_ = to_sibling[group]
        if group == "in":
            after = list(after) + [small["g_c_ctx"], small["total"]]
        full, from_sibling = _send_half_wait(sems_, full, lands_, after, "grads_to_sibling_wait_" + group)
        pairs = [_pair_sum(c_arr, f, r_, "pair_sum_" + nm) for f, r_, nm in zip(full, from_sibling, nms)]
        after = [small["g_c_ctx"], small["total"]] if group == "in" else []
        sems_, pairs, lands_, token_ = _scatter_start(pairs, "grads_to_owner_start_" + group, after)
        in_flight[group] = (sems_, pairs, lands_, token_)
        return token_[0, 0]

    r = _sample_back(reduce, front, x[0], ctx[0], loss_target[0], modc, modx, norm_pre1, norms, onorms, lb_full,
                     gla_side, w_in_r, wbh, wbg, wout, ffn_weights)
    grad_x = r["grad_x"]

    weights = dict(w_in=(w_in, m_w_in, v_w_in), w_br_hg=(w_br_hg, m_w_br_hg, v_w_br_hg),
                   w_br_gla=(w_br_gla, m_w_br_gla, v_w_br_gla), w_out=(w_out, m_w_out, v_w_out),
                   w_ff_gate=(w_ff_gate, m_w_ff_gate, v_w_ff_gate), w_ff_up=(w_ff_up, m_w_ff_up, v_w_ff_up),
                   w_ff_down=(w_ff_down, m_w_ff_down, v_w_ff_down))
    names = ["w_in", "w_br_hg", "w_br_gla", "w_out", "w_ff_gate", "w_ff_up", "w_ff_down"]
    big, swapping = {}, {}

    def sum_and_swap(group, after):
        sems_, pairs, lands_, _ = in_flight[group]
        pairs, lands_ = _scatter_wait(sems_, pairs, lands_, after, "grads_to_owner_wait_" + group)
        own_half = [_sum_owner(chip_arr, pr, g, "chip_sum_" + nm) for pr, g, nm in zip(pairs, lands_, groups[group])]
        swapping[group] = _swap_start(own_half, "halves_to_sibling_start_" + group)
        return own_half[-1]

    def update(group, after):
        sems_, own_half, lands_ = swapping[group]
        own_half, other_half = _swap_wait(sems_, own_half, lands_, after, "halves_to_sibling_wait_" + group)
        done = []
        for nm, own, oth in zip(groups[group], own_half, other_half):
            w_, m_, v_ = (view(a, nm) for a in weights[nm])
            res = _adamw_halves(c_arr, own, oth, w_, m_, v_, "adamw_" + nm)
            big[nm] = [r_.T[None] if nm in transposed else r_[None] for r_ in res]
            done.append(res[1])
        return done

    token_in = in_flight["in"][3]
    summed_ffn = sum_and_swap("ffn", [token_in])
    summed_mix = sum_and_swap("mix", [summed_ffn])

    total, g_b_mod, g_lb_full, g_w_mod, g_c_ctx = (small[k] for k in ("total", "g_b_mod", "g_lb_full", "g_w_mod",
                                                                      "g_c_ctx"))
    g_pre1, g_post1, g_pre2, g_post2 = (total[r_:r_ + 1] for r_ in (ROW_PRE1, ROW_POST1, ROW_PRE2, ROW_POST2))
    g_hg_on, g_gla_on = total[ROW_ONORM:ROW_ONORM + 1, 0:HD], total[ROW_ONORM:ROW_ONORM + 1, HD:2 * HD]
    n_lb = hg_lb.shape[2]
    g_hg_lb = lax.dynamic_slice(g_lb_full, (0, chip * n_lb), (4, n_lb))
    g_bgk = lax.dynamic_slice(total[ROW_BGK:ROW_BGK + 1].reshape(2, HW), (0, chip * n_lb), (2, n_lb))
    g_wgk_full = total[ROW_WGK:ROW_WGK + RANK].reshape(RANK, 2, HW).transpose(1, 0, 2).reshape(2 * RANK, HW)
    g_wgk = lax.dynamic_slice(g_wgk_full, (0, chip * n_lb), (2 * RANK, n_lb))

    small_items = [
        (g_c_ctx, c_ctx.reshape(1, d), m_c_ctx.reshape(1, d), v_c_ctx.reshape(1, d)),
        (g_b_mod, b_mod, m_b_mod, v_b_mod),
        (g_pre1, norm_pre1, m_norm_pre1, v_norm_pre1),
        (g_post1, norm_post1, m_norm_post1, v_norm_post1),
        (g_pre2, norm_pre2, m_norm_pre2, v_norm_pre2),
        (g_post2, norm_post2, m_norm_post2, v_norm_post2),
        (g_hg_lb, hg_lb.reshape(4, n_lb), m_hg_lb.reshape(4, n_lb), v_hg_lb.reshape(4, n_lb)),
        (g_hg_on, hg_onorm, m_hg_onorm, v_hg_onorm),
        (g_wgk, gla_w_gk.reshape(2 * RANK, n_lb), m_gla_w_gk.reshape(2 * RANK, n_lb), v_gla_w_gk.reshape(2 * RANK, n_lb)),
        (g_bgk, gla_b_gk.reshape(2, n_lb), m_gla_b_gk.reshape(2, n_lb), v_gla_b_gk.reshape(2, n_lb)),
        (g_gla_on, gla_onorm, m_gla_onorm, v_gla_onorm),
    ]
    small_res = _adamw_whole(small_items, "adamw_small")
    mod_res = _adamw_tiled(g_w_mod, w_mod[0], m_w_mod[0], v_w_mod[0], "adamw_w_mod")
    done_ffn = update("ffn", [summed_mix, mod_res[0], small_res[0][0]])
    done_mix = update("mix", done_ffn)
    update("in", [sum_and_swap("in", done_mix)])

    loss = total[ROW_LOSS, 0]

    shapes = dict(c_ctx=c_ctx.shape, b_mod=b_mod.shape, norm_pre1=norm_pre1.shape, norm_post1=norm_post1.shape,
                  norm_pre2=norm_pre2.shape, norm_post2=norm_post2.shape, hg_lb=hg_lb.shape, hg_onorm=hg_onorm.shape,
                  gla_w_gk=gla_w_gk.shape, gla_b_gk=gla_b_gk.shape, gla_onorm=gla_onorm.shape)
    small_names = ["c_ctx", "b_mod", "norm_pre1", "norm_post1", "norm_pre2", "norm_post2", "hg_lb", "hg_onorm",
                   "gla_w_gk", "gla_b_gk", "gla_onorm"]
    grads, deltas, new_m, new_v = {}, {}, {}, {}
    for nm, item, res in zip(small_names, small_items, small_res):
        grads[nm] = item[0].reshape(shapes[nm])
        deltas[nm], new_m[nm], new_v[nm] = (r_.reshape(shapes[nm]) for r_ in res)
    grads["w_mod"] = g_w_mod[None]
    deltas["w_mod"], new_m["w_mod"], new_v["w_mod"] = (r_[None] for r_ in mod_res)
    for nm in names:
        grads[nm], deltas[nm], new_m[nm], new_v[nm] = big[nm]
    order = ["c_ctx", "w_mod", "b_mod", "norm_pre1", "norm_post1", "norm_pre2", "norm_post2", "w_in", "hg_lb",
             "hg_onorm", "gla_w_gk", "gla_b_gk", "gla_onorm", "w_br_hg", "w_br_gla", "w_out", "w_ff_gate", "w_ff_up",
             "w_ff_down"]
    return (loss, grad_x[None], *[grads[n] for n in order], *[deltas[n] for n in order],
            *[new_m[n] for n in order], *[new_v[n] for n in order])
```

```python
import functools

import jax
import jax.numpy as jnp
from jax import lax
from jax.experimental import pallas as pl
from jax.experimental.pallas import tpu as pltpu

F32 = jnp.float32
BF16 = jnp.bfloat16
MESH = pl.DeviceIdType.MESH

EPS = 1e-6
CHUNK = 64
SUB = 16
NSUB = CHUNK // SUB
NH = 4
HD = 128
HW = NH * HD
RANK = 16
GATE_NORM = 16.0
N_MOD = 6
TM = 256
BWD_LANES = 8
N_DEV = 8
N_CHIP = 4
VMEM_LIMIT = 56 * 1024 * 1024

ADAM_LR = 0.001
ADAM_B1 = 0.9
ADAM_B2 = 0.999
ADAM_EPS = 1e-08
ADAM_WD = 0.01
ADAM_STEP = 10

VMEM_SPEC = pl.BlockSpec(memory_space=pltpu.VMEM)
ANY_SPEC = pl.BlockSpec(memory_space=pl.ANY)
HBM_SPEC = pl.BlockSpec(memory_space=pltpu.HBM)
SEM_SPEC = pl.BlockSpec(memory_space=pltpu.SEMAPHORE)
EFFECT = pltpu.SideEffectType.DATAFLOW_SIDE_EFFECTING


def _cparams(**kw):
    return pltpu.CompilerParams(vmem_limit_bytes=VMEM_LIMIT, **kw)


def _dot(a, b):
    return jnp.dot(a.astype(BF16), b.astype(BF16), preferred_element_type=F32)


def _dot_nt(a, b):
    return lax.dot_general(a.astype(BF16), b.astype(BF16), (((1,), (1,)), ((), ())), preferred_element_type=F32)


def _dot_tn(a, b):
    return lax.dot_general(a.astype(BF16), b.astype(BF16), (((0,), (0,)), ((), ())), preferred_element_type=F32)


def _sigmoid(x):
    return 1.0 / (1.0 + jnp.exp(-x))


def _silu(x):
    return x * _sigmoid(x)


def _dsilu(x):
    s = _sigmoid(x)
    return s * (1.0 + x * (1.0 - s))


def _log_sigmoid(x):
    return jnp.minimum(x, 0.0) - jnp.log(1.0 + jnp.exp(-jnp.abs(x)))


def _colsum(a):
    return jnp.sum(a, axis=0, keepdims=True)


def _rms(a):
    r = lax.rsqrt(jnp.mean(a * a, axis=-1, keepdims=True) + EPS)
    return a * r, r


def _rms_bwd(dn, n, r):
    return r * (dn - n * jnp.mean(dn * n, axis=-1, keepdims=True))


def _place():
    x, y, c = lax.axis_index("x"), lax.axis_index("y"), lax.axis_index("c")
    chips = [(1 - x, y), (x, 1 - y), (1 - x, 1 - y)]
    return x, y, c, chips


def _allgather8(v, name, after=()):
    rows, cols = v.shape
    n_after = len(after)

    def body(x_ref, *rest):
        out_ref, send_sems, recv_sems, local_sem = rest[n_after:]
        x, y, c, chips = _place()
        me, sibling = (x, y, c), (x, y, 1 - c)

        def blk(px, py, pc):
            return out_ref.at[4 * px + 2 * py + pc]

        def copy(k, block, to, src=None):
            return pltpu.make_async_remote_copy(
                src_ref=blk(*block) if src is None else src, dst_ref=blk(*block),
                send_sem=send_sems.at[k], recv_sem=recv_sems.at[k], device_id=to, device_id_type=MESH)

        mine = pltpu.make_async_copy(x_ref, blk(*me), local_sem)
        mine.start()
        first = [copy(0, me, sibling, src=x_ref)]
        first += [copy(1 + j, me, (*chip, c), src=x_ref) for j, chip in enumerate(chips)]
        for cp in first:
            cp.start()
        passed = [copy(4 + j, (*chip, c), sibling) for j, chip in enumerate(chips)]
        for j, chip in enumerate(chips):
            copy(1 + j, (*chip, c), me).wait_recv()
            passed[j].start()
        copy(0, sibling, me).wait_recv()
        for j, chip in enumerate(chips):
            copy(4 + j, (*chip, 1 - c), me).wait_recv()
        for cp in first + passed:
            cp.wait_send()
        mine.wait()

    return pl.pallas_call(
        body, name=name,
        out_shape=jax.ShapeDtypeStruct((N_DEV, rows, cols), v.dtype),
        in_specs=[VMEM_SPEC] + [ANY_SPEC] * n_after, out_specs=VMEM_SPEC,
        scratch_shapes=[pltpu.SemaphoreType.DMA((7,)), pltpu.SemaphoreType.DMA((7,)), pltpu.SemaphoreType.DMA],
    )(v, *after)


def _cast_into_blocks(chip_arr, w, name):
    rows, cols = w.shape
    tr = _row_tile(rows, 16, 256)

    def body(chip_ref, w_ref, o_ref):
        o_ref[0] = w_ref[...].astype(BF16)

    return pl.pallas_call(
        body, name=name,
        grid_spec=pltpu.PrefetchScalarGridSpec(
            num_scalar_prefetch=1, grid=(rows // tr,),
            in_specs=[pl.BlockSpec((tr, cols), lambda i, chip_ref: (i, 0))],
            out_specs=pl.BlockSpec((1, tr, cols), lambda i, chip_ref: (chip_ref[0], i, 0))),
        out_shape=jax.ShapeDtypeStruct((N_CHIP, rows, cols), BF16),
        compiler_params=_cparams(dimension_semantics=("parallel",)),
    )(chip_arr, w)


def _halved_by_rows(shape):
    return (shape[1] // 2) % 16 == 0


def _half_of(ref, pc, block=None):
    lead = slice(None) if block is None else block
    if _halved_by_rows(ref.shape):
        h = ref.shape[1] // 2
        return ref.at[lead, pl.ds(pl.multiple_of(pc * h, 16), h), :]
    h = ref.shape[2] // 2
    return ref.at[lead, :, pl.ds(pl.multiple_of(pc * h, 128), h)]


def _half_shape(shape):
    return (shape[0], shape[1] // 2, shape[2]) if _halved_by_rows(shape) else (shape[0], shape[1], shape[2] // 2)


def _half_rows(ref, chip_id, pc):
    return _half_of(ref, pc, chip_id)


def _hbm(a):
    return pltpu.with_memory_space_constraint(a, pltpu.HBM)


def _blocks_start(lands, name, after=()):
    n = len(lands)
    n_sem = 3 * n
    first = n + len(after)

    def body(*refs):
        lnd = refs[:n]
        send_sems, recv_sems = refs[first:first + n_sem], refs[first + n_sem:first + 2 * n_sem]
        token = refs[-1]
        x, y, c, chips = _place()
        me_chip = 2 * x + y
        for k in range(n):
            for j, chip in enumerate(chips):
                pltpu.make_async_remote_copy(
                    src_ref=_half_rows(lnd[k], me_chip, c), dst_ref=_half_rows(lnd[k], me_chip, c),
                    send_sem=send_sems[3 * k + j], recv_sem=recv_sems[3 * k + j],
                    device_id=(*chip, c), device_id_type=MESH).start()
        token[...] = jnp.zeros_like(token)

    out = pl.pallas_call(
        body, name=name,
        out_shape=(*[pltpu.SemaphoreType.DMA(())] * (2 * n_sem),
                   *[pltpu.HBM(l.shape, l.dtype) for l in lands],
                   jax.ShapeDtypeStruct((8, 128), F32)),
        in_specs=[HBM_SPEC] * n + [ANY_SPEC] * len(after),
        out_specs=(*[SEM_SPEC] * (2 * n_sem), *[HBM_SPEC] * n, VMEM_SPEC),
        input_output_aliases={i: 2 * n_sem + i for i in range(n)},
        compiler_params=pltpu.CompilerParams(has_side_effects=EFFECT),
    )(*[_hbm(l) for l in lands], *after)
    return list(out[:2 * n_sem]), list(out[2 * n_sem:2 * n_sem + n]), out[-1]


def _blocks_wait(sems, lands, after, name):
    n = len(lands)
    n_sem = 3 * n

    def body(*refs):
        lnd = refs[:n]
        s_sems, r_sems = refs[n:n + n_sem], refs[n + n_sem:n + 2 * n_sem]
        x, y, c, chips = _place()
        me_chip = 2 * x + y
        for k in range(n):
            for j, (px, py) in enumerate(chips):
                cp = pltpu.make_async_remote_copy(
                    src_ref=_half_rows(lnd[k], me_chip, c), dst_ref=_half_rows(lnd[k], 2 * px + py, c),
                    send_sem=s_sems[3 * k + j], recv_sem=r_sems[3 * k + j],
                    device_id=(px, py, c), device_id_type=MESH)
                cp.wait_send()
                cp.wait_recv()

    out = pl.pallas_call(
        body, name=name,
        out_shape=tuple(pltpu.HBM(l.shape, l.dtype) for l in lands),
        in_specs=[HBM_SPEC] * n + [SEM_SPEC] * (2 * n_sem) + [ANY_SPEC] * len(after),
        out_specs=[HBM_SPEC] * n,
        input_output_aliases={i: i for i in range(n)},
        compiler_params=pltpu.CompilerParams(has_side_effects=EFFECT),
    )(*lands, *sems, *after)
    return list(out)


def _forward_start(lands, name):
    n = len(lands)
    n_sem = 3 * n

    def body(*refs):
        lnd = refs[:n]
        send_sems, recv_sems = refs[n:n + n_sem], refs[n + n_sem:n + 2 * n_sem]
        x, y, c, chips = _place()
        for k in range(n):
            for j, (px, py) in enumerate(chips):
                pltpu.make_async_remote_copy(
                    src_ref=_half_rows(lnd[k], 2 * px + py, c), dst_ref=_half_rows(lnd[k], 2 * px + py, c),
                    send_sem=send_sems[3 * k + j], recv_sem=recv_sems[3 * k + j],
                    device_id=(x, y, 1 - c), device_id_type=MESH).start()
        refs[-1][...] = jnp.zeros_like(refs[-1])

    out = pl.pallas_call(
        body, name=name,
        out_shape=(*[pltpu.SemaphoreType.DMA(())] * (2 * n_sem), *[pltpu.HBM(l.shape, l.dtype) for l in lands],
                   jax.ShapeDtypeStruct((8, 128), F32)),
        in_specs=[HBM_SPEC] * n,
        out_specs=(*[SEM_SPEC] * (2 * n_sem), *[HBM_SPEC] * n, VMEM_SPEC),
        input_output_aliases={i: 2 * n_sem + i for i in range(n)},
        compiler_params=pltpu.CompilerParams(has_side_effects=EFFECT),
    )(*[_hbm(l) for l in lands])
    return list(out[:2 * n_sem]), list(out[2 * n_sem:2 * n_sem + n]), out[-1]


def _forward_wait(sems, lands, after, name):
    n = len(lands)
    n_sem = 3 * n

    def body(*refs):
        lnd = refs[:n]
        s_sems, r_sems = refs[n:n + n_sem], refs[n + n_sem:n + 2 * n_sem]
        x, y, c, chips = _place()
        for k in range(n):
            for j, (px, py) in enumerate(chips):
                cp = pltpu.make_async_remote_copy(
                    src_ref=_half_rows(lnd[k], 2 * px + py, c), dst_ref=_half_rows(lnd[k], 2 * px + py, 1 - c),
                    send_sem=s_sems[3 * k + j], recv_sem=r_sems[3 * k + j],
                    device_id=(x, y, 1 - c), device_id_type=MESH)
                cp.wait_send()
                cp.wait_recv()

    out = pl.pallas_call(
        body, name=name,
        out_shape=tuple(pltpu.HBM(l.shape, l.dtype) for l in lands),
        in_specs=[HBM_SPEC] * n + [SEM_SPEC] * (2 * n_sem) + [ANY_SPEC] * len(after),
        out_specs=[HBM_SPEC] * n,
        input_output_aliases={i: i for i in range(n)},
        compiler_params=pltpu.CompilerParams(has_side_effects=EFFECT),
    )(*lands, *sems, *after)
    return list(out)


def _blocks_finish(lands, name):
    n = len(lands)

    def body(*refs):
        lnd = refs[n:2 * n]
        send_sems, recv_sems = refs[2 * n:]
        x, y, c, chips = _place()
        sibling = (x, y, 1 - c)

        def copy(k, j, chip_id, pc):
            return pltpu.make_async_remote_copy(
                src_ref=_half_rows(lnd[k], chip_id, pc), dst_ref=_half_rows(lnd[k], chip_id, pc),
                send_sem=send_sems.at[k, j], recv_sem=recv_sems.at[k, j], device_id=sibling, device_id_type=MESH)

        started = []
        for k in range(n):
            for j, (px, py) in enumerate(chips):
                cp = copy(k, j, 2 * px + py, c)
                cp.start()
                started.append(cp)
        for k in range(n):
            for j, (px, py) in enumerate(chips):
                copy(k, j, 2 * px + py, 1 - c).wait_recv()
        for cp in started:
            cp.wait_send()

    out = pl.pallas_call(
        body, name=name,
        out_shape=[jax.ShapeDtypeStruct(l.shape, l.dtype) for l in lands],
        in_specs=[ANY_SPEC] * n, out_specs=[ANY_SPEC] * n,
        input_output_aliases={i: i for i in range(n)},
        scratch_shapes=[pltpu.SemaphoreType.DMA((n, 3)), pltpu.SemaphoreType.DMA((n, 3))],
    )(*lands)
    return list(out)


def _send_half_start(arrs, name):
    n = len(arrs)

    def body(*refs):
        ins, lnd = refs[:n], refs[n:2 * n]
        send_sems, recv_sems = refs[2 * n:3 * n], refs[3 * n:4 * n]
        token = refs[-1]
        x, y, c, _ = _place()
        for k in range(n):
            pltpu.make_async_remote_copy(
                src_ref=_half_of(ins[k], 1 - c), dst_ref=lnd[k], send_sem=send_sems[k], recv_sem=recv_sems[k],
                device_id=(x, y, 1 - c), device_id_type=MESH).start()
        token[...] = jnp.zeros_like(token)

    lands = [_hbm(lax.empty(_half_shape(a.shape), a.dtype)) for a in arrs]
    out = pl.pallas_call(
        body, name=name,
        out_shape=(*[pltpu.SemaphoreType.DMA(())] * (2 * n), *[pltpu.HBM(a.shape, a.dtype) for a in arrs],
                   *[pltpu.HBM(l.shape, l.dtype) for l in lands], jax.ShapeDtypeStruct((8, 128), F32)),
        in_specs=[HBM_SPEC] * (2 * n),
        out_specs=(*[SEM_SPEC] * (2 * n), *[HBM_SPEC] * (2 * n), VMEM_SPEC),
        input_output_aliases={i: 2 * n + i for i in range(2 * n)},
        compiler_params=pltpu.CompilerParams(has_side_effects=EFFECT),
    )(*[_hbm(a) for a in arrs], *lands)
    return list(out[:2 * n]), list(out[2 * n:3 * n]), list(out[3 * n:4 * n]), out[-1]


def _send_half_wait(sems, arrs, lands, after, name):
    n = len(arrs)

    def body(*refs):
        ins, lnd = refs[:n], refs[n:2 * n]
        s_sems, r_sems = refs[2 * n:3 * n], refs[3 * n:4 * n]
        x, y, c, _ = _place()
        for k in range(n):
            cp = pltpu.make_async_remote_copy(
                src_ref=_half_of(ins[k], 1 - c), dst_ref=lnd[k], send_sem=s_sems[k], recv_sem=r_sems[k],
                device_id=(x, y, 1 - c), device_id_type=MESH)
            cp.wait_send()
            cp.wait_recv()

    out = pl.pallas_call(
        body, name=name,
        out_shape=tuple(pltpu.HBM(a.shape, a.dtype) for a in list(arrs) + list(lands)),
        in_specs=[HBM_SPEC] * (2 * n) + [SEM_SPEC] * (2 * n) + [ANY_SPEC] * len(after),
        out_specs=[HBM_SPEC] * (2 * n),
        input_output_aliases={i: i for i in range(2 * n)},
        compiler_params=pltpu.CompilerParams(has_side_effects=EFFECT),
    )(*arrs, *lands, *sems, *after)
    return list(out[:n]), list(out[n:])


def _scatter_start(arrs, name, after=()):
    n = len(arrs)
    n_sem = 3 * n
    first = 2 * n + len(after)

    def body(*refs):
        ins, lnd = refs[:n], refs[n:2 * n]
        send_sems, recv_sems = refs[first:first + n_sem], refs[first + n_sem:first + 2 * n_sem]
        token = refs[-1]
        x, y, c, chips = _place()
        me_chip = 2 * x + y
        for k in range(n):
            for j, (px, py) in enumerate(chips):
                pltpu.make_async_remote_copy(
                    src_ref=ins[k].at[2 * px + py], dst_ref=lnd[k].at[me_chip],
                    send_sem=send_sems[3 * k + j], recv_sem=recv_sems[3 * k + j],
                    device_id=(px, py, c), device_id_type=MESH).start()
        token[...] = jnp.zeros_like(token)

    lands = [_hbm(lax.empty(a.shape, a.dtype)) for a in arrs]
    out = pl.pallas_call(
        body, name=name,
        out_shape=(*[pltpu.SemaphoreType.DMA(())] * (2 * n_sem),
                   *[pltpu.HBM(a.shape, a.dtype) for a in arrs], *[pltpu.HBM(a.shape, a.dtype) for a in arrs],
                   jax.ShapeDtypeStruct((8, 128), F32)),
        in_specs=[HBM_SPEC] * (2 * n) + [ANY_SPEC] * len(after),
        out_specs=(*[SEM_SPEC] * (2 * n_sem), *[HBM_SPEC] * (2 * n), VMEM_SPEC),
        input_output_aliases={i: 2 * n_sem + i for i in range(2 * n)},
        compiler_params=pltpu.CompilerParams(has_side_effects=EFFECT),
    )(*[_hbm(a) for a in arrs], *lands, *after)
    base = 2 * n_sem
    return list(out[:base]), list(out[base:base + n]), list(out[base + n:base + 2 * n]), out[-1]


def _scatter_wait(sems, arrs, lands, after, name):
    n = len(arrs)
    n_sem = 3 * n

    def body(*refs):
        ins, lnd = refs[:n], refs[n:2 * n]
        s_sems, r_sems = refs[2 * n:2 * n + n_sem], refs[2 * n + n_sem:2 * n + 2 * n_sem]
        x, y, c, chips = _place()
        for k in range(n):
            for j, (px, py) in enumerate(chips):
                cp = pltpu.make_async_remote_copy(
                    src_ref=ins[k].at[2 * px + py], dst_ref=lnd[k].at[2 * px + py],
                    send_sem=s_sems[3 * k + j], recv_sem=r_sems[3 * k + j],
                    device_id=(px, py, c), device_id_type=MESH)
                cp.wait_send()
                cp.wait_recv()

    out = pl.pallas_call(
        body, name=name,
        out_shape=tuple(pltpu.HBM(a.shape, a.dtype) for a in list(arrs) + list(lands)),
        in_specs=[HBM_SPEC] * (2 * n) + [SEM_SPEC] * (2 * n_sem) + [ANY_SPEC] * len(after),
        out_specs=[HBM_SPEC] * (2 * n),
        input_output_aliases={i: i for i in range(2 * n)},
        compiler_params=pltpu.CompilerParams(has_side_effects=EFFECT),
    )(*arrs, *lands, *sems, *after)
    return list(out[:n]), list(out[n:])


def _sum_owner(chip_arr, pairs, got, name):
    nb, h, cols = got.shape
    tr = _row_tile(h, 16, 256)

    def body(chip_ref, own_ref, a_ref, b_ref, c_ref, o_ref):
        o_ref[...] = ((own_ref[0].astype(F32) + a_ref[0].astype(F32)) + b_ref[0].astype(F32)) + c_ref[0].astype(F32)

    def slot(off):
        return pl.BlockSpec((1, tr, cols), lambda i, chip_ref: ((chip_ref[0] + off) % N_CHIP, i, 0))

    return pl.pallas_call(
        body, name=name,
        grid_spec=pltpu.PrefetchScalarGridSpec(
            num_scalar_prefetch=1, grid=(h // tr,),
            in_specs=[slot(0), slot(1), slot(2), slot(3)],
            out_specs=pl.BlockSpec((tr, cols), lambda i, chip_ref: (i, 0))),
        out_shape=jax.ShapeDtypeStruct((h, cols), F32),
        compiler_params=_cparams(dimension_semantics=("parallel",)),
    )(chip_arr, pairs, got, got, got)


def _swap_start(arrs, name, after=()):
    n = len(arrs)
    first = 2 * n + len(after)

    def body(*refs):
        ins, lnd = refs[:n], refs[n:2 * n]
        send_sems, recv_sems = refs[first:first + n], refs[first + n:first + 2 * n]
        x, y, c, _ = _place()
        for k in range(n):
            pltpu.make_async_remote_copy(
                src_ref=ins[k], dst_ref=lnd[k], send_sem=send_sems[k], recv_sem=recv_sems[k],
                device_id=(x, y, 1 - c), device_id_type=MESH).start()

    lands = [_hbm(lax.empty(a.shape, a.dtype)) for a in arrs]
    out = pl.pallas_call(
        body, name=name,
        out_shape=(*[pltpu.SemaphoreType.DMA(())] * (2 * n), *[pltpu.HBM(a.shape, a.dtype) for a in arrs],
                   *[pltpu.HBM(a.shape, a.dtype) for a in arrs]),
        in_specs=[HBM_SPEC] * (2 * n) + [ANY_SPEC] * len(after),
        out_specs=(*[SEM_SPEC] * (2 * n), *[HBM_SPEC] * (2 * n)),
        input_output_aliases={i: 2 * n + i for i in range(2 * n)},
        compiler_params=pltpu.CompilerParams(has_side_effects=EFFECT),
    )(*[_hbm(a) for a in arrs], *lands, *after)
    return list(out[:2 * n]), list(out[2 * n:3 * n]), list(out[3 * n:4 * n])


def _swap_wait(sems, arrs, lands, after, name):
    n = len(arrs)

    def body(*refs):
        ins, lnd = refs[:n], refs[n:2 * n]
        s_sems, r_sems = refs[2 * n:3 * n], refs[3 * n:4 * n]
        x, y, c, _ = _place()
        for k in range(n):
            cp = pltpu.make_async_remote_copy(
                src_ref=ins[k], dst_ref=lnd[k], send_sem=s_sems[k], recv_sem=r_sems[k],
                device_id=(x, y, 1 - c), device_id_type=MESH)
            cp.wait_send()
            cp.wait_recv()

    out = pl.pallas_call(
        body, name=name,
        out_shape=tuple(pltpu.HBM(a.shape, a.dtype) for a in list(arrs) + list(lands)),
        in_specs=[HBM_SPEC] * (2 * n) + [SEM_SPEC] * (2 * n) + [ANY_SPEC] * len(after),
        out_specs=[HBM_SPEC] * (2 * n),
        input_output_aliases={i: i for i in range(2 * n)},
        compiler_params=pltpu.CompilerParams(has_side_effects=EFFECT),
    )(*arrs, *lands, *sems, *after)
    return list(out[:n]), list(out[n:])


def _row_tile(h, mult=8, cap=128):
    for t in range(cap - cap % mult, mult - 1, -mult):
        if h % t == 0:
            return t
    if mult > 8:
        return _row_tile(h, 8, cap)
    raise ValueError(h)


def _pair_sum(c_arr, full, recv, name):
    nb, rows, cols = full.shape

    def body(c_ref, f_ref, r_ref, o_ref):
        o_ref[...] = (f_ref[...] + r_ref[...]).astype(BF16)

    if _halved_by_rows(full.shape):
        h = rows // 2
        tr = _row_tile(h, 16, 256)
        steps = h // tr
        own = pl.BlockSpec((1, tr, cols), lambda b, i, c_ref: (b, c_ref[0] * steps + i, 0))
        half = pl.BlockSpec((1, tr, cols), lambda b, i, c_ref: (b, i, 0))
    else:
        steps = 1
        own = pl.BlockSpec((1, rows, cols // 2), lambda b, i, c_ref: (b, 0, c_ref[0]))
        half = pl.BlockSpec((1, rows, cols // 2), lambda b, i, c_ref: (b, 0, 0))
    return pl.pallas_call(
        body, name=name,
        grid_spec=pltpu.PrefetchScalarGridSpec(
            num_scalar_prefetch=1, grid=(nb, steps), in_specs=[own, half], out_specs=half),
        out_shape=jax.ShapeDtypeStruct(_half_shape(full.shape), BF16),
        compiler_params=_cparams(dimension_semantics=("parallel", "parallel")),
    )(c_arr, full, recv)


def _adam_math(g, w, m, v):
    m1 = ADAM_B1 * m + (1.0 - ADAM_B1) * g
    v1 = ADAM_B2 * v + (1.0 - ADAM_B2) * (g * g)
    m_hat = m1 / (1.0 - ADAM_B1 ** ADAM_STEP)
    v_hat = v1 / (1.0 - ADAM_B2 ** ADAM_STEP)
    delta = -ADAM_LR * (m_hat / (jnp.sqrt(v_hat) + ADAM_EPS) + ADAM_WD * w)
    return delta, m1, v1


def _adamw_halves(c_arr, own, other, w, m, v, name):
    rows, cols = w.shape
    by_rows = own.shape[1] == cols

    def body(c_ref, own_ref, oth_ref, w_ref, m_ref, v_ref, g_out, d_out, m_out, v_out):
        if by_rows:
            g = jnp.where(pl.program_id(0) == c_ref[0], own_ref[...], oth_ref[...])
        else:
            own_, oth_ = own_ref[...], oth_ref[...]
            g = jnp.where(c_ref[0] == 0, jnp.concatenate([own_, oth_], axis=1), jnp.concatenate([oth_, own_], axis=1))
        d, m1, v1 = _adam_math(g, w_ref[...], m_ref[...], v_ref[...])
        g_out[...] = g
        d_out[...] = d
        m_out[...] = m1
        v_out[...] = v1

    if by_rows:
        h = rows // 2
        tr = _row_tile(h)
        steps = h // tr
        grid = (2, steps)
        half_spec = pl.BlockSpec((tr, cols), lambda p, i, c_ref: (i, 0))
        full_spec = pl.BlockSpec((tr, cols), lambda p, i, c_ref: (p * steps + i, 0))
    else:
        tr = _row_tile(rows)
        grid = (1, rows // tr)
        half_spec = pl.BlockSpec((tr, cols // 2), lambda p, i, c_ref: (i, 0))
        full_spec = pl.BlockSpec((tr, cols), lambda p, i, c_ref: (i, 0))
    return pl.pallas_call(
        body, name=name,
        grid_spec=pltpu.PrefetchScalarGridSpec(
            num_scalar_prefetch=1, grid=grid,
            in_specs=[half_spec, half_spec, full_spec, full_spec, full_spec],
            out_specs=[full_spec] * 4),
        out_shape=[jax.ShapeDtypeStruct(w.shape, F32)] * 4,
        compiler_params=_cparams(dimension_semantics=("parallel", "parallel")),
    )(c_arr, own, other, w, m, v)


def _adamw_whole(items, name):
    n = len(items)

    def body(*refs):
        ins, outs = refs[:4 * n], refs[4 * n:]
        for k in range(n):
            g, w, m, v = (r[...] for r in ins[4 * k:4 * k + 4])
            d, m1, v1 = _adam_math(g, w, m, v)
            outs[3 * k][...] = d
            outs[3 * k + 1][...] = m1
            outs[3 * k + 2][...] = v1

    flat = [a for it in items for a in it]
    shapes = [jax.ShapeDtypeStruct(it[1].shape, F32) for it in items for _ in range(3)]
    out = pl.pallas_call(
        body, name=name, out_shape=shapes,
        in_specs=[VMEM_SPEC] * (4 * n), out_specs=[VMEM_SPEC] * (3 * n),
        compiler_params=_cparams(),
    )(*flat)
    return [tuple(out[3 * k:3 * k + 3]) for k in range(n)]


def _adamw_tiled(g, w, m, v, name):
    rows, cols = w.shape
    tr = _row_tile(rows)

    def body(g_ref, w_ref, m_ref, v_ref, d_out, m_out, v_out):
        d, m1, v1 = _adam_math(g_ref[...], w_ref[...], m_ref[...], v_ref[...])
        d_out[...] = d
        m_out[...] = m1
        v_out[...] = v1

    spec = pl.BlockSpec((tr, cols), lambda i: (i, 0))
    return pl.pallas_call(
        body, name=name, grid=(rows // tr,),
        out_shape=[jax.ShapeDtypeStruct(w.shape, F32)] * 3,
        in_specs=[spec] * 4, out_specs=[spec] * 3,
        compiler_params=_cparams(dimension_semantics=("parallel",)),
    )(g, w, m, v)


def _mod_forward(cond, w_mod, b_mod_cols, name):
    def body(c_ref, w_ref, b_ref, o_ref):
        o_ref[...] = _dot(_silu(c_ref[...]), w_ref[...]) + b_ref[...]

    return pl.pallas_call(
        body, name=name, out_shape=jax.ShapeDtypeStruct((cond.shape[0], w_mod.shape[1]), F32),
        in_specs=[VMEM_SPEC] * 3, out_specs=VMEM_SPEC, compiler_params=_cparams(),
    )(cond, w_mod, b_mod_cols)


def _mod_backward(cond, w_mod, dmod_cols, name):
    def body(c_ref, w_ref, d_ref, gw_ref, gc_ref):
        s = _silu(c_ref[...])
        d = d_ref[...]
        gw_ref[...] = _dot_tn(s, d)
        gc_ref[...] = _dot_nt(d[8:16, :], w_ref[...])

    return pl.pallas_call(
        body, name=name,
        out_shape=[jax.ShapeDtypeStruct(w_mod.shape, F32), jax.ShapeDtypeStruct((8, w_mod.shape[0]), F32)],
        in_specs=[VMEM_SPEC] * 3, out_specs=[VMEM_SPEC] * 2, compiler_params=_cparams(),
    )(cond, w_mod, dmod_cols)


def _col_chunks(width, step=512):
    return [(s, min(step, width - s)) for s in range(0, width, step)]


def _w_in_row(p_off):
    if p_off < 9 * HW:
        return p_off
    return 9 * HW if p_off == OFF_LR else p_off + 2 * RANK


def _in_projection(ctx0, x0, modc, modx, pre1, w_t, n_ctx_tiles, name):
    d = x0.shape[1]
    rows = ctx0.shape[0] + x0.shape[0]
    width = P_WIDTH

    def body(ctx_ref, x_ref, modc_ref, modx_ref, pre_ref, w_ref, h_ref, p_ref):
        is_ctx = pl.program_id(0) < n_ctx_tiles
        n, _ = _rms(jnp.where(is_ctx, ctx_ref[...], x_ref[...]))
        shift = jnp.where(is_ctx, modc_ref[0:1, :], modx_ref[0:1, :])
        scale = jnp.where(is_ctx, modc_ref[1:2, :], modx_ref[1:2, :])
        h = (n * pre_ref[...] * (1.0 + scale) + shift).astype(BF16)
        h_ref[...] = h
        for s, w in _col_chunks(width):
            p_ref[:, s:s + w] = _dot_nt(h, w_ref[_w_in_row(s):_w_in_row(s) + w, :])

    row = lambda i: (i, 0)
    fixed = lambda i: (0, 0)
    return pl.pallas_call(
        body, name=name, grid=(rows // TM,),
        out_shape=[jax.ShapeDtypeStruct((rows, d), BF16), jax.ShapeDtypeStruct((rows, width), F32)],
        in_specs=[pl.BlockSpec((TM, d), lambda i: (jnp.minimum(i, n_ctx_tiles - 1), 0)),
                  pl.BlockSpec((TM, d), lambda i: (jnp.maximum(i - n_ctx_tiles, 0), 0)),
                  pl.BlockSpec((8, d), fixed), pl.BlockSpec((8, d), fixed), pl.BlockSpec((1, d), fixed), VMEM_SPEC],
        out_specs=[pl.BlockSpec((TM, d), row), pl.BlockSpec((TM, width), row)],
        compiler_params=_cparams(dimension_semantics=("parallel",)),
    )(ctx0, x0, modc, modx, pre1, w_t)


C_HQ, C_HI, C_HF_FW, C_HF_BW, C_HGATE, C_GQ, C_GK, C_GV, C_GGATE = range(9)
OFF_GATE_HG = 9 * HW
OFF_LR = 13 * HW
P_WIDTH = OFF_LR + 128


def _head_norm_fwd(o, w):
    outs, ns, rs = [], [], []
    for h in range(NH):
        n, r = _rms(o[:, h * HD:(h + 1) * HD])
        ns.append(n)
        rs.append(r)
        outs.append(n * w)
    return jnp.concatenate(outs, axis=1), ns, rs


def _mixer_tail(z, o_hg, o_gla, p_hgate, p_ggate, p_gate_hg, p_gate_gla, hg_on, gla_on, wbh, wbg, wout):
    on_hg, n_hg, r_hg = _head_norm_fwd(o_hg, hg_on)
    on_gla, n_gla, r_gla = _head_norm_fwd(o_gla, gla_on)
    og_hg = (on_hg * _silu(p_hgate)).astype(BF16)
    og_gla = (on_gla * _silu(p_ggate)).astype(BF16)
    b_hg = jnp.dot(og_hg, wbh, preferred_element_type=F32)
    b_gla = jnp.dot(og_gla, wbg, preferred_element_type=F32)
    s_hg = _sigmoid(p_gate_hg)
    s_gla = _sigmoid(p_gate_gla)
    merged = (s_hg * b_hg + s_gla * b_gla).astype(BF16)
    y1 = jnp.dot(merged, wout, preferred_element_type=F32)
    return dict(on_hg=on_hg, n_hg=n_hg, r_hg=r_hg, on_gla=on_gla, n_gla=n_gla, r_gla=r_gla, og_hg=og_hg,
                og_gla=og_gla, b_hg=b_hg, b_gla=b_gla, s_hg=s_hg, s_gla=s_gla, merged=merged, y1=y1)


def _mixer_tail_fwd(x_lat, p, o_list, modx, norms, onorms, w_br_hg, w_br_gla, w_out, n_ctx_tiles, name):
    rows, d = x_lat.shape

    def body(x_ref, ofw_hg, obw_hg, ofw_gla, obw_gla, p_hgate, p_ggate, p_ghg_a, p_ghg_b, p_ggla_a, p_ggla_b,
             modx_ref, norm_ref, on_ref, wbh_ref, wbg_ref, wout_ref, z2_ref, y1_ref, mrg_ref, oghg_ref, oggla_ref):
        p_gate_hg = jnp.concatenate([p_ghg_a[...], p_ghg_b[...]], axis=1)
        p_gate_gla = jnp.concatenate([p_ggla_a[...], p_ggla_b[...]], axis=1)
        t = _mixer_tail(x_ref[...], ofw_hg[...] + obw_hg[...], ofw_gla[...] + obw_gla[...], p_hgate[...],
                        p_ggate[...], p_gate_hg, p_gate_gla, on_ref[0:1, 0:HD], on_ref[1:2, 0:HD],
                        wbh_ref[...], wbg_ref[...], wout_ref[...])
        y1_ref[...] = t["y1"]
        mrg_ref[...] = t["merged"]
        oghg_ref[...] = t["og_hg"]
        oggla_ref[...] = t["og_gla"]
        n1, _ = _rms(t["y1"])
        z2_ref[...] = x_ref[...] + n1 * norm_ref[1:2, :] * modx_ref[2:3, :]

    lat = lambda i: (i, 0)
    full = lambda i: (i + n_ctx_tiles, 0)
    fixed = lambda i: (0, 0)

    def pcol(blk):
        return pl.BlockSpec((TM, HW), lambda i: (i + n_ctx_tiles, blk))

    in_specs = ([pl.BlockSpec((TM, d), lat)] + [pl.BlockSpec((TM, HW), full)] * 4
                + [pcol(C_HGATE), pcol(C_GGATE), pcol(9), pcol(10), pcol(11), pcol(12)]
                + [pl.BlockSpec((8, d), fixed)] * 3 + [VMEM_SPEC] * 3)
    bf = lambda w: jax.ShapeDtypeStruct((rows, w), BF16)
    f32 = jax.ShapeDtypeStruct((rows, d), F32)
    return pl.pallas_call(
        body, name=name, grid=(rows // TM,), out_shape=[f32, f32, bf(d), bf(HW), bf(HW)], in_specs=in_specs,
        out_specs=[pl.BlockSpec((TM, d), lat)] * 3 + [pl.BlockSpec((TM, HW), lat)] * 2,
        compiler_params=_cparams(dimension_semantics=("parallel",)),
    )(x_lat, *o_list, p, p, p, p, p, p, modx, norms, onorms, w_br_hg, w_br_gla, w_out)


def _ffn_fwd_bwd(z2, modx, norms, w_gate, w_up, w_down, target, name):
    rows, d = z2.shape
    dff = w_gate.shape[0]
    inv_d = 1.0 / d

    def body(z2_ref, modx_ref, norm_ref, wg_ref, wu_ref, wd_ref, t_ref,
             loss_ref, dz2_ref, h2_ref, a_ref, du_ref, dv_ref, dy2_ref, stat_ref):
        i = pl.program_id(0)
        pre2, post2 = norm_ref[2:3, :], norm_ref[3:4, :]
        shift2, scale2, gate2 = modx_ref[3:4, :], modx_ref[4:5, :], modx_ref[5:6, :]
        z2 = z2_ref[...]
        n2, r2 = _rms(z2)
        nw2 = n2 * pre2
        h2 = (nw2 * (1.0 + scale2) + shift2).astype(BF16)
        h2_ref[...] = h2
        u = _dot_nt(h2, wg_ref[...])
        v = _dot_nt(h2, wu_ref[...])
        su = _silu(u)
        a = (su * v).astype(BF16)
        a_ref[...] = a
        y2 = jnp.dot(a, wd_ref[...], preferred_element_type=F32)
        n3, r3 = _rms(y2)
        err = z2 + n3 * post2 * gate2 - t_ref[...]
        part = 0.5 * inv_d * jnp.sum(err * err)
        dz3 = err * inv_d
        dgate2 = _colsum(dz3 * n3 * post2)
        tt = dz3 * gate2
        dpost2 = _colsum(tt * n3)
        dy2 = _rms_bwd(tt * post2, n3, r3).astype(BF16)
        dy2_ref[...] = dy2
        da = _dot_nt(dy2, wd_ref[...])
        du = (da * v * _dsilu(u)).astype(BF16)
        dv = (da * su).astype(BF16)
        du_ref[...] = du
        dv_ref[...] = dv
        dh2 = (jnp.dot(du, wg_ref[...], preferred_element_type=F32)
               + jnp.dot(dv, wu_ref[...], preferred_element_type=F32))
        dshift2 = _colsum(dh2)
        dscale2 = _colsum(dh2 * nw2)
        dnw2 = dh2 * (1.0 + scale2)
        dpre2 = _colsum(dnw2 * n2)
        dz2_ref[...] = dz3 + _rms_bwd(dnw2 * pre2, n2, r2)

        @pl.when(i == 0)
        def _():
            stat_ref[...] = jnp.zeros_like(stat_ref)
            loss_ref[...] = jnp.zeros_like(loss_ref)

        for r, val in enumerate((dshift2, dscale2, dgate2, dpre2, dpost2)):
            stat_ref[r:r + 1, :] += val
        loss_ref[...] += part
        stat_ref[5:6, 0:128] += part

    lat = lambda i: (i, 0)
    fixed = lambda i: (0, 0)
    bf = lambda w: jax.ShapeDtypeStruct((rows, w), BF16)
    return pl.pallas_call(
        body, name=name, grid=(rows // TM,),
        out_shape=[jax.ShapeDtypeStruct((8, 128), F32), jax.ShapeDtypeStruct((rows, d), F32), bf(d), bf(dff), bf(dff),
                   bf(dff), bf(d), jax.ShapeDtypeStruct((8, d), F32)],
        in_specs=[pl.BlockSpec((TM, d), lat), pl.BlockSpec((8, d), fixed), pl.BlockSpec((8, d), fixed)]
        + [VMEM_SPEC] * 3 + [pl.BlockSpec((TM, d), lat)],
        out_specs=[pl.BlockSpec((8, 128), fixed), pl.BlockSpec((TM, d), lat), pl.BlockSpec((TM, d), lat),
                   pl.BlockSpec((TM, dff), lat), pl.BlockSpec((TM, dff), lat), pl.BlockSpec((TM, dff), lat),
                   pl.BlockSpec((TM, d), lat), pl.BlockSpec((8, d), fixed)],
        compiler_params=_cparams(dimension_semantics=("arbitrary",)),
    )(z2, modx, norms, w_gate, w_up, w_down, target)


def _mixer_tail_bwd(x_lat, p, o_list, dz2, y1, modx, norms, onorms, w_br_hg, w_br_gla, w_out, n_ctx_tiles, n_tiles,
                    name):
    rows, d = x_lat.shape
    total = n_tiles * TM

    def body(x_ref, ofw_hg, obw_hg, ofw_gla, obw_gla, p_hgate, p_ggate, p_ghg_a, p_ghg_b, p_ggla_a, p_ggla_b,
             dz2_ref, y1_ref, modx_ref, norm_ref, on_ref, wbh_ref, wbg_ref, wout_ref,
             dohg_ref, dogla_ref, dhgate_ref, dggate_ref, dghg_ref, dggla_ref, dy1_ref, dbhg_ref, dbgla_ref,
             stat_ref):
        i = pl.program_id(0)

        @pl.when(i == 0)
        def _():
            stat_ref[...] = jnp.zeros_like(stat_ref)

        @pl.when(i < n_ctx_tiles)
        def _():
            for ref in (dohg_ref, dogla_ref, dhgate_ref, dggate_ref, dghg_ref, dggla_ref):
                ref[...] = jnp.zeros_like(ref)

        @pl.when(i >= n_ctx_tiles)
        def _():
            post1, gate1 = norm_ref[1:2, :], modx_ref[2:3, :]
            hg_on, gla_on = on_ref[0:1, 0:HD], on_ref[1:2, 0:HD]
            p_gate_hg = jnp.concatenate([p_ghg_a[...], p_ghg_b[...]], axis=1)
            p_gate_gla = jnp.concatenate([p_ggla_a[...], p_ggla_b[...]], axis=1)
            ph, pg = p_hgate[...], p_ggate[...]
            t = _mixer_tail(x_ref[...], ofw_hg[...] + obw_hg[...], ofw_gla[...] + obw_gla[...], ph, pg,
                            p_gate_hg, p_gate_gla, hg_on, gla_on, wbh_ref[...], wbg_ref[...], wout_ref[...])
            dz2 = dz2_ref[...]
            n1, r1 = _rms(y1_ref[...])
            dgate1 = _colsum(dz2 * n1 * post1)
            tt = dz2 * gate1
            dpost1 = _colsum(tt * n1)
            dy1 = _rms_bwd(tt * post1, n1, r1).astype(BF16)
            dy1_ref[...] = dy1
            dmerged = _dot_nt(dy1, wout_ref[...])
            dghg_ref[...] = (dmerged * t["b_hg"] * t["s_hg"] * (1.0 - t["s_hg"])).astype(BF16)
            dggla_ref[...] = (dmerged * t["b_gla"] * t["s_gla"] * (1.0 - t["s_gla"])).astype(BF16)
            db_hg = (dmerged * t["s_hg"]).astype(BF16)
            db_gla = (dmerged * t["s_gla"]).astype(BF16)
            dbhg_ref[...] = db_hg
            dbgla_ref[...] = db_gla
            don_acc = []
            for (db, wb, pgate, on, ns, rs, gain, gate_ref, do_ref) in (
                    (db_hg, wbh_ref, ph, t["on_hg"], t["n_hg"], t["r_hg"], hg_on, dhgate_ref, dohg_ref),
                    (db_gla, wbg_ref, pg, t["on_gla"], t["n_gla"], t["r_gla"], gla_on, dggate_ref, dogla_ref)):
                dog = _dot_nt(db, wb[...])
                gate_ref[...] = (dog * on * _dsilu(pgate)).astype(BF16)
                don = dog * _silu(pgate)
                acc = jnp.zeros((1, HD), F32)
                for h in range(NH):
                    sl = slice(h * HD, (h + 1) * HD)
                    acc = acc + _colsum(don[:, sl] * ns[h])
                    do_ref[:, sl] = _rms_bwd(don[:, sl] * gain, ns[h], rs[h]).astype(BF16)
                don_acc.append(acc)
            stat_ref[0:1, :] += dgate1
            stat_ref[1:2, :] += dpost1
            stat_ref[2:3, 0:HD] += don_acc[0]
            stat_ref[2:3, HD:2 * HD] += don_acc[1]

    lat = lambda i: (jnp.maximum(i - n_ctx_tiles, 0), 0)
    full = lambda i: (i, 0)
    fixed = lambda i: (0, 0)

    def pcol(blk):
        return pl.BlockSpec((TM, HW), lambda i: (i, blk))

    in_specs = ([pl.BlockSpec((TM, d), lat)] + [pl.BlockSpec((TM, HW), full)] * 4
                + [pcol(C_HGATE), pcol(C_GGATE), pcol(9), pcol(10), pcol(11), pcol(12)]
                + [pl.BlockSpec((TM, d), lat), pl.BlockSpec((TM, d), lat)]
                + [pl.BlockSpec((8, d), fixed)] * 3 + [VMEM_SPEC] * 3)
    f = lambda w: jax.ShapeDtypeStruct((total, w), BF16)
    out_shape = [f(HW), f(HW), f(HW), f(HW), f(d), f(d), jax.ShapeDtypeStruct((rows, d), BF16),
                 jax.ShapeDtypeStruct((rows, d), BF16), jax.ShapeDtypeStruct((rows, d), BF16),
                 jax.ShapeDtypeStruct((8, d), F32)]
    out_specs = ([pl.BlockSpec((TM, HW), full)] * 4 + [pl.BlockSpec((TM, d), full)] * 2
                 + [pl.BlockSpec((TM, d), lat)] * 3 + [pl.BlockSpec((8, d), fixed)])
    return pl.pallas_call(
        body, name=name, grid=(n_tiles,), out_shape=out_shape, in_specs=in_specs, out_specs=out_specs,
        compiler_params=_cparams(dimension_semantics=("arbitrary",)),
    )(x_lat, *o_list, p, p, p, p, p, p, dz2, y1, modx, norms, onorms, w_br_hg, w_br_gla, w_out)


def _in_projection_bwd(ctx0, x0, dz2, modc, modx, pre1, w_t, pieces, n_ctx_tiles, name):
    d = x0.shape[1]
    rows = ctx0.shape[0] + x0.shape[0]
    lat_rows = dz2.shape[0]
    width = P_WIDTH
    n_pieces = len(pieces)

    def body(*refs):
        ctx_ref, x_ref, dz2_ref, modc_ref, modx_ref, pre_ref, w_ref = refs[:7]
        (dhq_f, dhq_b, dhi_f, dhi_b, dhf_f, dhf_b, dhgate, dgq_f, dgq_b, dgk_f, dgk_b, dgv_f, dgv_b, dggate,
         dghg, dggla, dlr_f, dlr_b) = refs[7:7 + n_pieces]
        dp_ref, gx_ref, stat_ref = refs[7 + n_pieces:]
        i = pl.program_id(0)
        is_ctx = i < n_ctx_tiles
        z = jnp.where(is_ctx, ctx_ref[...], x_ref[...])
        sections = [
            (0, dhq_f[...] + dhq_b[...]), (HW, dhi_f[...] + dhi_b[...]), (2 * HW, dhf_f[...]), (3 * HW, dhf_b[...]),
            (4 * HW, dhgate[...]), (5 * HW, dgq_f[...] + dgq_b[...]), (6 * HW, dgk_f[...] + dgk_b[...]),
            (7 * HW, dgv_f[...] + dgv_b[...]), (8 * HW, dggate[...]),
            (9 * HW, dghg[:, 0:HW]), (10 * HW, dghg[:, HW:2 * HW]),
            (11 * HW, dggla[:, 0:HW]), (12 * HW, dggla[:, HW:2 * HW]), (OFF_LR, dlr_f[...] + dlr_b[...])]
        dh = jnp.zeros((TM, d), F32)
        for off, val in sections:
            w = val.shape[1]
            vb = val.astype(BF16)
            dp_ref[off:off + w, :] = vb.T
            dh = dh + jnp.dot(vb, w_ref[_w_in_row(off):_w_in_row(off) + w, :], preferred_element_type=F32)
        n, r = _rms(z)
        pre = pre_ref[...]
        scale = jnp.where(is_ctx, modc_ref[1:2, :], modx_ref[1:2, :])
        nw = n * pre
        dshift = _colsum(dh)
        dscale = _colsum(dh * nw)
        dnw = dh * (1.0 + scale)
        dpre = _colsum(dnw * n)
        gx_ref[...] = dz2_ref[...] + _rms_bwd(dnw * pre, n, r)
        zero = jnp.zeros((1, d), F32)

        @pl.when(i == 0)
        def _():
            stat_ref[...] = jnp.zeros_like(stat_ref)

        stat_ref[0:1, :] += jnp.where(is_ctx, zero, dshift)
        stat_ref[1:2, :] += jnp.where(is_ctx, zero, dscale)
        stat_ref[2:3, :] += jnp.where(is_ctx, dshift, zero)
        stat_ref[3:4, :] += jnp.where(is_ctx, dscale, zero)
        stat_ref[4:5, :] += dpre

    full = lambda i: (i, 0)
    lat = lambda i: (jnp.maximum(i - n_ctx_tiles, 0), 0)
    fixed = lambda i: (0, 0)
    piece_specs = [pl.BlockSpec((TM, a.shape[1]), full) for a in pieces]
    in_specs = [pl.BlockSpec((TM, d), lambda i: (jnp.minimum(i, n_ctx_tiles - 1), 0)), pl.BlockSpec((TM, d), lat),
                pl.BlockSpec((TM, d), lat), pl.BlockSpec((8, d), fixed),
                pl.BlockSpec((8, d), fixed), pl.BlockSpec((1, d), fixed), VMEM_SPEC] + piece_specs
    return pl.pallas_call(
        body, name=name, grid=(rows // TM,),
        out_shape=[jax.ShapeDtypeStruct((width, rows), BF16), jax.ShapeDtypeStruct((lat_rows, d), F32),
                   jax.ShapeDtypeStruct((8, d), F32)],
        in_specs=in_specs,
        out_specs=[pl.BlockSpec((width, TM), lambda i: (0, i)), pl.BlockSpec((TM, d), lat),
                   pl.BlockSpec((8, d), fixed)],
        compiler_params=_cparams(dimension_semantics=("arbitrary",)),
    )(ctx0, x0, dz2, modc, modx, pre1, w_t, *pieces)


def _transposed_lhs_matmul(x_ref, dy_ref, o_ref, xt_ref):
    @pl.when(pl.program_id(1) == 0)
    def _():
        xt_ref[...] = x_ref[...].T

    o_ref[...] = jnp.dot(xt_ref[...], dy_ref[...], preferred_element_type=F32)


def _w_in_grad(dp_t, h1, n_cols, name):
    rows, d = h1.shape
    n_main = OFF_LR // HW
    lr0 = _w_in_row(OFF_LR)

    def body(x_ref, xlr_ref, h_ref, o_hbm, acc_ref, sems):
        i = pl.program_id(0)
        slot = i % 2

        def main_copy(step):
            row = jnp.where(step < 9, step * HW, step * HW + 2 * RANK)
            return pltpu.make_async_copy(acc_ref.at[step % 2], o_hbm.at[pl.ds(pl.multiple_of(row, 8), HW), :],
                                         sems.at[step % 2])

        @pl.when(i > 1)
        def _():
            main_copy(i - 2).wait()

        @pl.when(i < n_main)
        def _():
            acc_ref[slot] = jnp.dot(x_ref[...], h_ref[...], preferred_element_type=F32)
            main_copy(i).start()

        @pl.when(i == n_main)
        def _():
            acc_ref[slot, 0:128, :] = jnp.dot(xlr_ref[...], h_ref[...], preferred_element_type=F32)
            lr_copy = pltpu.make_async_copy(acc_ref.at[slot, 0:2 * RANK, :], o_hbm.at[lr0:lr0 + 2 * RANK, :],
                                            sems.at[slot])
            lr_copy.start()
            main_copy(i - 1).wait()
            lr_copy.wait()

    return pl.pallas_call(
        body, name=name, grid=(n_main + 1,),
        out_shape=jax.ShapeDtypeStruct((n_cols, d), F32),
        in_specs=[pl.BlockSpec((HW, rows), lambda i: (jnp.minimum(i, n_main - 1), 0)),
                  pl.BlockSpec((128, rows), lambda i: (OFF_LR // 128, 0)),
                  pl.BlockSpec((rows, d), lambda i: (0, 0))],
        out_specs=ANY_SPEC,
        scratch_shapes=[pltpu.VMEM((2, HW, d), F32), pltpu.SemaphoreType.DMA((2,))],
        compiler_params=_cparams(dimension_semantics=("arbitrary",)),
    )(dp_t, dp_t, h1)


def _weight_grad(xs, dy, name, tk=None, tn=512, k_first=0, k_tiles=None):
    rows = dy.shape[0]
    n = dy.shape[1]
    tn_ = min(tn, n)
    tk_ = xs.shape[1] if tk is None else tk
    k_tiles = xs.shape[1] // tk_ if k_tiles is None else k_tiles
    k = k_tiles * tk_

    return pl.pallas_call(
        functools.partial(_transposed_lhs_matmul), name=name, grid=(k_tiles, n // tn_),
        out_shape=jax.ShapeDtypeStruct((k, n), F32),
        in_specs=[pl.BlockSpec((rows, tk_), lambda i, j: (0, i + k_first)),
                  pl.BlockSpec((rows, tn_), lambda i, j: (0, j))],
        out_specs=pl.BlockSpec((tk_, tn_), lambda i, j: (i, j)),
        scratch_shapes=[pltpu.VMEM((tk_, rows), BF16)],
        compiler_params=_cparams(dimension_semantics=("parallel", "arbitrary")),
    )(xs, dy)


def _running_sums(xs, fws):
    c = xs[0].shape[0]
    row = lax.broadcasted_iota(jnp.int32, (c, 1), 0)
    s = 1
    while s < c:
        xs = [x + (jnp.where(row >= s, pltpu.roll(x, s, axis=0), 0.0) if fw else
                   jnp.where(row < c - s, pltpu.roll(x, c - s, axis=0), 0.0)) for x, fw in zip(xs, fws)]
        s *= 2
    return xs


def _chunks_terms(qs, ks, gs, fws):
    c = CHUNK
    n = len(qs)
    r = lax.broadcasted_iota(jnp.int32, (c, c), 0)
    s = lax.broadcasted_iota(jnp.int32, (c, c), 1)
    row = lax.broadcasted_iota(jnp.int32, (c, 1), 0)
    per_dir = {}
    for fw in set(fws):
        pos = row if fw else (c - 1 - row)
        per_dir[fw] = dict(
            causal=(s <= r) if fw else (s >= r), causal_t=(s >= r) if fw else (s <= r), pos=pos,
            in_blk=[(pos >= SUB * j) & (pos < SUB * (j + 1)) for j in range(NSUB)],
            start_row=[None] + [SUB * j - 1 if fw else c - SUB * j for j in range(1, NSUB)],
            rend=c - 1 if fw else 0)
    dirs = [per_dir[fw] for fw in fws]
    cums = _running_sums(gs, fws)
    starts = [[None] + [cum[d["start_row"][j]:d["start_row"][j] + 1, :] for j in range(1, NSUB)]
              for cum, d in zip(cums, dirs)]
    es = [[jnp.exp(cum) for cum in cums]]
    for j in range(1, NSUB):
        es.append([jnp.exp(jnp.where(d["pos"] >= SUB * j, cum - st[j], -1e30)) for cum, st, d in zip(cums, starts, dirs)])
    owns = [functools.reduce(lambda rest, j: jnp.where(d["in_blk"][j], st[j], rest), range(1, NSUB), 0.0)
            for st, d in zip(starts, dirs)]
    kscales = [jnp.exp(own - cum) for own, cum in zip(owns, cums)]
    cends = [cum[d["rend"]:d["rend"] + 1, :] for cum, d in zip(cums, dirs)]
    tails = [jnp.exp(cend - cum) for cend, cum in zip(cends, cums)]
    qcats = [jnp.concatenate([q * es[j][i] for j in range(NSUB)], axis=1).astype(BF16) for i, q in enumerate(qs)]
    kts = [k * ksc for k, ksc in zip(ks, kscales)]
    kms = [jnp.concatenate([jnp.where(d["in_blk"][j], kt, 0.0) for j in range(NSUB)], axis=1).astype(BF16)
           for kt, d in zip(kts, dirs)]
    e_by_lane = [[es[j][i] for j in range(NSUB)] for i in range(n)]
    return dict(dirs=dirs, e=e_by_lane, kscale=kscales, cend=cends, tail=tails, qcat=qcats, km=kms, kt=kts)


def _chunks_fwd(qs, ks, vs, gs, st0s, fws):
    t = _chunks_terms(qs, ks, gs, fws)
    scores = [_dot_nt(qc, km) for qc, km in zip(t["qcat"], t["km"])]
    a = [jnp.where(d["causal"], sc, 0.0) for sc, d in zip(scores, t["dirs"])]
    inter = [_dot_nt(qc[:, 0:HD], st0) for qc, st0 in zip(t["qcat"], st0s)]
    intra = [_dot(a_, v) for a_, v in zip(a, vs)]
    os_ = [x + y for x, y in zip(intra, inter)]
    upd = [_dot_tn(v, k * tl) for v, k, tl in zip(vs, ks, t["tail"])]
    st1s = [st0 * jnp.exp(ce) + u for st0, ce, u in zip(st0s, t["cend"], upd)]
    return os_, st1s


def _chunks_bwd(qs, ks, vs, gs, st0s, dos, dst1s, fws):
    n = len(qs)
    t = _chunks_terms(qs, ks, gs, fws)
    qcat, km, e, dirs = t["qcat"], t["km"], t["e"], t["dirs"]
    a_t = [jnp.where(d["causal_t"], _dot_nt(km_, qc), 0.0) for km_, qc, d in zip(km, qcat, dirs)]
    ktail = [k * tl for k, tl in zip(ks, t["tail"])]
    dv_a = [_dot(at, do) for at, do in zip(a_t, dos)]
    dv_b = [_dot_nt(kt, ds) for kt, ds in zip(ktail, dst1s)]
    dv = [x + y for x, y in zip(dv_a, dv_b)]
    da = [jnp.where(d["causal"], _dot_nt(do, v), 0.0) for do, v, d in zip(dos, vs, dirs)]
    da_t = [jnp.where(d["causal_t"], _dot_nt(v, do), 0.0) for do, v, d in zip(dos, vs, dirs)]
    dqcat = [_dot(da_, km_) for da_, km_ in zip(da, km)]
    dq_inter = [e[i][0] * _dot(dos[i], st0s[i]) for i in range(n)]
    dkm = [_dot(dat, qc) for dat, qc in zip(da_t, qcat)]
    dk_inter = [_dot(v, ds) * tl for v, ds, tl in zip(vs, dst1s, t["tail"])]
    dq = [dq_inter[i] + sum(e[i][j] * dqcat[i][:, j * HD:(j + 1) * HD] for j in range(NSUB)) for i in range(n)]
    dkt = [sum(jnp.where(dirs[i]["in_blk"][j], dkm[i][:, j * HD:(j + 1) * HD], 0.0) for j in range(NSUB))
           for i in range(n)]
    dk = [dkt[i] * t["kscale"][i] + dk_inter[i] for i in range(n)]
    dcum = [qs[i] * dq_inter[i] - ks[i] * dk_inter[i] - t["kt"][i].astype(BF16).astype(F32) * dkt[i]
            + sum(qcat[i][:, j * HD:(j + 1) * HD].astype(F32) * dqcat[i][:, j * HD:(j + 1) * HD] for j in range(NSUB))
            for i in range(n)]
    ecend = [jnp.exp(ce) for ce in t["cend"]]
    end = [ecend[i] * _colsum(st0s[i] * dst1s[i]) + _colsum(ks[i] * dk_inter[i]) for i in range(n)]
    sums = _running_sums(dcum, [not fw for fw in fws])
    dg = [sm + en for sm, en in zip(sums, end)]
    upd = [_dot_tn(dos[i], qs[i] * e[i][0]) for i in range(n)]
    dst0 = [dst1s[i] * ecend[i] + upd[i] for i in range(n)]
    return dq, dk, dv, dg, dst0


def _chunk_index(step, n_ctx_chunks, n_chunks, fw):
    if fw:
        return step
    return jnp.where(step < n_ctx_chunks, n_ctx_chunks - 1 - step, n_chunks - 1 + n_ctx_chunks - step)


def _hg_inputs(hq, hf, lbv, d_idx, sl):
    lb = _sigmoid(lbv[d_idx:d_idx + 1, sl] - lbv[2 + d_idx:3 + d_idx, sl])
    sg = _sigmoid(hf)
    f = lb + (1.0 - lb) * sg
    return _silu(hq), 1.0 - f, jnp.log(f), f, sg, lb


def _scan_fwd_both(p, branch_sides, n_ctx_chunks, name):
    rows = p.shape[0]
    n_chunks = rows // CHUNK
    n_ins = [4 if branch == "hg" else 6 for branch, _ in branch_sides]
    n_in_all = 2 * sum(n_ins)
    n_br = len(branch_sides)

    def body(*refs):
        ins, outs, state = refs[:n_in_all], refs[n_in_all:n_in_all + 4 * n_br], refs[-1]

        @pl.when(pl.program_id(0) == 0)
        def _():
            state[...] = jnp.zeros_like(state)

        lanes, where = [], []
        pos = 0
        for bi, (branch, _) in enumerate(branch_sides):
            hg = branch == "hg"
            n_in = n_ins[bi]
            for di, fw in enumerate((True, False)):
                r = ins[pos:pos + n_in]
                pos += n_in
                o_ref, st_ref = outs[4 * bi + 2 * di], outs[4 * bi + 2 * di + 1]
                if hg:
                    a_ref, b_ref, c_ref, lb_ref = r
                else:
                    a_ref, b_ref, c_ref, lr_ref, wgk_ref, bgk_ref = r
                    logits = _dot(lr_ref[...], wgk_ref[...]) + bgk_ref[...]
                    g_all = _log_sigmoid(logits) * (1.0 / GATE_NORM)
                for h in range(NH):
                    sl = slice(h * HD, (h + 1) * HD)
                    if hg:
                        q, k, g, _, _, _ = _hg_inputs(a_ref[:, sl], c_ref[:, sl], lb_ref[...], di, sl)
                        v = b_ref[:, sl]
                    else:
                        q, k, v, g = a_ref[:, sl] * (HD ** -0.5), b_ref[:, sl], c_ref[:, sl], g_all[:, sl]
                    lanes.append((q, k, v, g, state[2 * bi + di, h], fw))
                    where.append((2 * bi + di, h, sl, o_ref, st_ref))
        qs, ks, vs, gs, st0s, fws = (list(col) for col in zip(*lanes))
        os_, st1s = _chunks_fwd(qs, ks, vs, gs, st0s, fws)
        for (si, h, sl, o_ref, st_ref), st0, o, st1 in zip(where, st0s, os_, st1s):
            st_ref[0, h] = st0
            o_ref[:, sl] = o
            state[si, h] = st1

    fixed = lambda j: (0, 0)
    in_specs, args, out_specs = [], [], []
    for branch, side in branch_sides:
        for di, fw in enumerate((True, False)):
            chunk = functools.partial(_chunk_index, n_ctx_chunks=n_ctx_chunks, n_chunks=n_chunks, fw=fw)

            def cmap(blk, width=HW, chunk=chunk):
                return pl.BlockSpec((CHUNK, width), lambda j: (chunk(j), blk))

            if branch == "hg":
                in_specs += [cmap(C_HQ), cmap(C_HI), cmap(C_HF_FW + di), pl.BlockSpec((4, HW), fixed)]
                args += [p, p, p, side]
            else:
                in_specs += [cmap(C_GQ), cmap(C_GK), cmap(C_GV), cmap(OFF_LR // 128, 128),
                             pl.BlockSpec((128, HW), fixed), pl.BlockSpec((1, HW), fixed)]
                args += [p, p, p, p, side[di][0], side[di][1]]
            out_specs += [cmap(0), pl.BlockSpec((1, NH, HD, HD), lambda j, chunk=chunk: (chunk(j), 0, 0, 0))]
    return pl.pallas_call(
        body, name=name, grid=(n_chunks,),
        out_shape=[jax.ShapeDtypeStruct((rows, HW), F32),
                   jax.ShapeDtypeStruct((n_chunks, NH, HD, HD), F32)] * (2 * n_br),
        in_specs=in_specs, out_specs=out_specs,
        scratch_shapes=[pltpu.VMEM((2 * n_br, NH, HD, HD), F32)],
        compiler_params=_cparams(dimension_semantics=("arbitrary",)),
    )(*args)


def _scan_bwd_both(p, branch_items, n_ctx_chunks, name):
    rows = p.shape[0]
    n_chunks = rows // CHUNK
    n_ins = [6 if item[0] == "hg" else 8 for item in branch_items]
    n_outs = [4 if item[0] == "hg" else 6 for item in branch_items]
    n_in_all, n_out_all = 2 * sum(n_ins), 2 * sum(n_outs)

    def body(*refs):
        ins, outs, dstate = refs[:n_in_all], refs[n_in_all:n_in_all + n_out_all], refs[-1]
        first = pl.program_id(0) == 0

        @pl.when(first)
        def _():
            dstate[...] = jnp.zeros_like(dstate)

        lanes, where, extra, ctx = [], [], [], []
        ipos = opos = 0
        for bi, item in enumerate(branch_items):
            hg = item[0] == "hg"
            for di, fw in enumerate((True, False)):
                r, w = ins[ipos:ipos + n_ins[bi]], outs[opos:opos + n_outs[bi]]
                ipos += n_ins[bi]
                opos += n_outs[bi]
                if hg:
                    a_ref, b_ref, c_ref, lb_ref, st_ref, do_ref = r
                    acc_refs = (w[3],)
                else:
                    a_ref, b_ref, c_ref, lr_ref, wgk_ref, bgk_ref, st_ref, do_ref = r
                    acc_refs = (w[4], w[5])
                    lr = lr_ref[...]
                    logits = _dot(lr, wgk_ref[...]) + bgk_ref[...]
                    g_all = _log_sigmoid(logits) * (1.0 / GATE_NORM)

                @pl.when(first)
                def _(acc_refs=acc_refs):
                    for ref in acc_refs:
                        ref[...] = jnp.zeros_like(ref)

                for h in range(NH):
                    sl = slice(h * HD, (h + 1) * HD)
                    if hg:
                        hq, hf = a_ref[:, sl], c_ref[:, sl]
                        q, k, g, f, sg, lb = _hg_inputs(hq, hf, lb_ref[...], di, sl)
                        v = b_ref[:, sl]
                        extra.append((hq, f, sg, lb))
                    else:
                        q, k, v, g = a_ref[:, sl] * (HD ** -0.5), b_ref[:, sl], c_ref[:, sl], g_all[:, sl]
                        extra.append(None)
                    lanes.append((q, k, v, g, st_ref[0, h], do_ref[:, sl], dstate[2 * bi + di, h], fw))
                    where.append((2 * bi + di, h, sl))
                ctx.append((hg, w, None if hg else (lr, logits, wgk_ref)))

        dqs, dks, dvs, dgs, dst0s = [], [], [], [], []
        for lo in range(0, len(lanes), BWD_LANES):
            cols = [list(col) for col in zip(*lanes[lo:lo + BWD_LANES])]
            for acc, part in zip((dqs, dks, dvs, dgs, dst0s), _chunks_bwd(*cols)):
                acc.extend(part)
        dg_parts = [[] for _ in ctx]
        for (si, h, sl), ex, dq, dk, dv, dg, dst0 in zip(where, extra, dqs, dks, dvs, dgs, dst0s):
            dstate[si, h] = dst0
            hg, w, _ = ctx[si]
            if hg:
                hq, f, sg, lb = ex
                da_ref, db_ref, dc_ref, dlb_ref = w
                da_ref[:, sl] = (dq * _dsilu(hq)).astype(BF16)
                db_ref[:, sl] = dv.astype(BF16)
                df = dg / f - dk
                dc_ref[:, sl] = (df * (1.0 - lb) * sg * (1.0 - sg)).astype(BF16)
                dlb_ref[0:1, sl] += _colsum(df * (1.0 - sg))
            else:
                da_ref, db_ref, dc_ref = w[:3]
                da_ref[:, sl] = (dq * (HD ** -0.5)).astype(BF16)
                db_ref[:, sl] = dk.astype(BF16)
                dc_ref[:, sl] = dv.astype(BF16)
                dg_parts[si].append(dg)
        for si, (hg, w, more) in enumerate(ctx):
            if not hg:
                dlr_ref, dwgk_ref, dbias_ref = w[3:]
                lr, logits, wgk_ref = more
                dlogits = jnp.concatenate(dg_parts[si], axis=1) * (1.0 / GATE_NORM) * (1.0 - _sigmoid(logits))
                dlr_ref[...] = _dot_nt(dlogits, wgk_ref[...]).astype(BF16)
                dwgk_ref[...] += _dot_tn(lr, dlogits)
                dbias_ref[0:1, :] += _colsum(dlogits)

    fixed = lambda j: (0, 0)
    big = jax.ShapeDtypeStruct((rows, HW), BF16)
    in_specs, args, out_shape, out_specs = [], [], [], []
    for branch, side, states, d_o in branch_items:
        for di, fw in enumerate((True, False)):
            def chunk_of(j, fw=fw):
                return _chunk_index(n_chunks - 1 - j, n_ctx_chunks, n_chunks, fw)

            def cmap(blk, width=HW, chunk_of=chunk_of):
                return pl.BlockSpec((CHUNK, width), lambda j: (chunk_of(j), blk))

            st_spec = pl.BlockSpec((1, NH, HD, HD), lambda j, chunk_of=chunk_of: (chunk_of(j), 0, 0, 0))
            if branch == "hg":
                in_specs += [cmap(C_HQ), cmap(C_HI), cmap(C_HF_FW + di), pl.BlockSpec((4, HW), fixed), st_spec,
                             cmap(0)]
                args += [p, p, p, side, states[di], d_o]
                out_shape += [big, big, big, jax.ShapeDtypeStruct((8, HW), F32)]
                out_specs += [cmap(0), cmap(0), cmap(0), pl.BlockSpec((8, HW), fixed)]
            else:
                in_specs += [cmap(C_GQ), cmap(C_GK), cmap(C_GV), cmap(OFF_LR // 128, 128),
                             pl.BlockSpec((128, HW), fixed), pl.BlockSpec((1, HW), fixed), st_spec, cmap(0)]
                args += [p, p, p, p, side[di][0], side[di][1], states[di], d_o]
                out_shape += [big, big, big, jax.ShapeDtypeStruct((rows, 128), BF16),
                              jax.ShapeDtypeStruct((128, HW), F32), jax.ShapeDtypeStruct((8, HW), F32)]
                out_specs += [cmap(0), cmap(0), cmap(0), cmap(0, 128), pl.BlockSpec((128, HW), fixed),
                              pl.BlockSpec((8, HW), fixed)]
    return pl.pallas_call(
        body, name=name, grid=(n_chunks,), out_shape=out_shape, in_specs=in_specs, out_specs=out_specs,
        scratch_shapes=[pltpu.VMEM((2 * len(branch_items), NH, HD, HD), F32)],
        compiler_params=_cparams(dimension_semantics=("arbitrary",)),
    )(*args)


SMALL_ROWS = 56
ROWS_MOD_X = (0, 1, 8, 16, 17, 18)
ROWS_MOD_C = (2, 3)
ROW_PRE1, ROW_POST1, ROW_ONORM, ROW_PRE2, ROW_POST2, ROW_LB, ROW_BGK, ROW_WGK = 4, 9, 10, 19, 20, 24, 32, 40
ROW_LOSS = 21


def _reduce_small(gathered, lb_full, name):
    _, _, d = gathered.shape

    def body(g_ref, lb_ref, sum_ref, dmod_ref, dbmod_ref, dlb_ref):
        total = g_ref[0]
        for b in range(1, N_DEV):
            total = total + g_ref[b]
        sum_ref[...] = total
        dmod_ref[...] = jnp.zeros_like(dmod_ref)
        for m in range(N_MOD):
            col = slice(m * d, (m + 1) * d)
            acc = jnp.zeros((1, d), F32)
            for b in range(N_DEV):
                row = g_ref[b, ROWS_MOD_X[m]:ROWS_MOD_X[m] + 1, :]
                dmod_ref[b:b + 1, col] = row
                acc = acc + row
            if m < 2:
                ctx_row = total[ROWS_MOD_C[m]:ROWS_MOD_C[m] + 1, :]
                dmod_ref[8:9, col] = ctx_row
                acc = acc + ctx_row
            dbmod_ref[:, col] = acc
        lbv = lb_ref[...]
        for dd in range(2):
            lb = _sigmoid(lbv[dd:dd + 1, :] - lbv[2 + dd:3 + dd, :])
            gl = total[ROW_LB:ROW_LB + 1, dd * HW:(dd + 1) * HW] * lb * (1.0 - lb)
            dlb_ref[dd:dd + 1, :] = gl
            dlb_ref[2 + dd:3 + dd, :] = -gl

    return pl.pallas_call(
        body, name=name,
        out_shape=[jax.ShapeDtypeStruct((SMALL_ROWS, d), F32), jax.ShapeDtypeStruct((16, N_MOD * d), F32),
                   jax.ShapeDtypeStruct((1, N_MOD * d), F32), jax.ShapeDtypeStruct((4, HW), F32)],
        in_specs=[VMEM_SPEC] * 2, out_specs=[VMEM_SPEC] * 4, compiler_params=_cparams(),
    )(gathered, lb_full)


def _c_ctx_grad(gathered, c_ctx_row, name):
    def body(g_ref, c_ref, o_ref):
        acc = g_ref[0, 0:1, :]
        for chip in range(1, N_CHIP):
            acc = acc + g_ref[2 * chip, 0:1, :]
        o_ref[...] = acc * _dsilu(c_ref[...])

    return pl.pallas_call(
        body, name=name, out_shape=jax.ShapeDtypeStruct(c_ctx_row.shape, F32),
        in_specs=[VMEM_SPEC] * 2, out_specs=VMEM_SPEC, compiler_params=_cparams(),
    )(gathered, c_ctx_row)


def _blocked(full, n_blocks):
    k, n = full.shape
    return full.reshape(k, n_blocks, n // n_blocks).transpose(1, 0, 2)


def _unblocked(blocks):
    nb, k, n = blocks.shape
    return blocks.transpose(1, 0, 2).reshape(k, nb * n)


def _sample_front(x0, ctx0, modc, modx, norm_pre1, lb_full, gla_side, w_in_r):
    ctx_len = ctx0.shape[0]
    n_ctx_tiles = ctx_len // TM
    n_ctx_chunks = ctx_len // CHUNK
    h1, p = _in_projection(ctx0, x0, modc, modx, norm_pre1, w_in_r, n_ctx_tiles, "in_projection")
    (o_hg_fw, st_hg_fw, o_hg_bw, st_hg_bw, o_gla_fw, st_gla_fw, o_gla_bw, st_gla_bw) = _scan_fwd_both(
        p, [("hg", lb_full), ("gla", gla_side)], n_ctx_chunks, "scan_fwd")
    return dict(h1=h1, p=p, o_list=[o_hg_fw, o_hg_bw, o_gla_fw, o_gla_bw],
                states=[st_hg_fw, st_hg_bw, st_gla_fw, st_gla_bw])


def _sample_back(reduce, front, x0, ctx0, target0, modc, modx, norm_pre1, norms, onorms, lb_full, gla_side, w_in_r,
                 wbh, wbg, wout, ffn_weights):
    seq, d = x0.shape
    ctx_len = ctx0.shape[0]
    n_ctx_tiles = ctx_len // TM
    n_tiles = (ctx_len + seq) // TM
    n_ctx_chunks = ctx_len // CHUNK
    h1, p, o_list = front["h1"], front["p"], front["o_list"]
    st_hg_fw, st_hg_bw, st_gla_fw, st_gla_bw = front["states"]
    z2, y1, merged, og_hg, og_gla = _mixer_tail_fwd(x0, p, o_list, modx, norms, onorms, wbh, wbg, wout, n_ctx_tiles,
                                                    "mixer_tail")
    wg, wu, wd = ffn_weights([z2])
    loss_part, dz2, h2, a_act, du, dv, dy2, stat_ffn = _ffn_fwd_bwd(z2, modx, norms, wg, wu, wd, target0, "ffn")
    dff = wg.shape[0]
    tok = reduce("ffn", [_weight_grad(du, h2, "grad_w_ff_gate", tk=dff // 2, tn=d),
                         _weight_grad(dv, h2, "grad_w_ff_up", tk=dff // 2, tn=d),
                         _weight_grad(a_act, dy2, "grad_w_ff_down", tk=dff // 2)])

    (d_ohg, d_ogla, d_hgate, d_ggate, d_ghg, d_ggla, dy1, db_hg, db_gla, stat_mix) = _mixer_tail_bwd(
        x0, p, o_list, dz2, y1, modx + tok, norms, onorms, wbh, wbg, wout, n_ctx_tiles, n_tiles, "mixer_tail_bwd")
    tok = reduce("mix", [_weight_grad(og_hg, db_hg, "grad_w_br_hg"), _weight_grad(og_gla, db_gla, "grad_w_br_gla"),
                         _weight_grad(merged, dy1, "grad_w_out")])
    tok = tok + reduce("push_ffn", [dy1])
    gla_b = [(wgk, bias + tok) for wgk, bias in gla_side]
    (dgq_f, dgk_f, dgv_f, dlr_f, dwgk_f, dbgk_f, dgq_b, dgk_b, dgv_b, dlr_b, dwgk_b, dbgk_b,
     dhq_f, dhi_f, dhf_f, dlb_f, dhq_b, dhi_b, dhf_b, dlb_b) = _scan_bwd_both(
        p, [("gla", gla_b, (st_gla_fw, st_gla_bw), d_ogla), ("hg", lb_full, (st_hg_fw, st_hg_bw), d_ohg)],
        n_ctx_chunks, "scan_bwd")
    tok = reduce("push_mix", [dbgk_f])
    pieces = [dhq_f, dhq_b, dhi_f, dhi_b, dhf_f, dhf_b, d_hgate, dgq_f, dgq_b, dgk_f, dgk_b, dgv_f, dgv_b, d_ggate,
              d_ghg, d_ggla, dlr_f, dlr_b]
    dp, grad_x, stat_in = _in_projection_bwd(ctx0, x0, dz2, modc, modx, norm_pre1 + tok, w_in_r, pieces, n_ctx_tiles,
                                             "in_projection_bwd")

    tok = reduce("in", [_w_in_grad(dp, h1, w_in_r.shape[0], "grad_w_in")])
    reduce("small", dict(stat_in=stat_in + tok, stat_mix=stat_mix, stat_ffn=stat_ffn, dlb=(dlb_f, dlb_b),
                         dwgk=(dwgk_f, dwgk_b), dbgk=(dbgk_f, dbgk_b)))
    reduce("push_in", [])
    return dict(loss_part=loss_part, grad_x=grad_x)


def kernel(x, c, ctx, c_ctx, w_mod, b_mod, norm_pre1, norm_post1, norm_pre2, norm_post2, w_in, hg_lb, hg_onorm, gla_w_gk, gla_b_gk, gla_onorm, w_br_hg, w_br_gla, w_out, w_ff_gate, w_ff_up, w_ff_down, loss_target, m_c_ctx, m_w_mod, m_b_mod, m_norm_pre1, m_norm_post1, m_norm_pre2, m_norm_post2, m_w_in, m_hg_lb, m_hg_onorm, m_gla_w_gk, m_gla_b_gk, m_gla_onorm, m_w_br_hg, m_w_br_gla, m_w_out, m_w_ff_gate, m_w_ff_up, m_w_ff_down, v_c_ctx, v_w_mod, v_b_mod, v_norm_pre1, v_norm_post1, v_norm_pre2, v_norm_post2, v_w_in, v_hg_lb, v_hg_onorm, v_gla_w_gk, v_gla_b_gk, v_gla_onorm, v_w_br_hg, v_w_br_gla, v_w_out, v_w_ff_gate, v_w_ff_up, v_w_ff_down):
    seq, d = x.shape[1], x.shape[2]
    ctx_len = ctx.shape[1]
    assert seq % TM == 0 and ctx_len % TM == 0 and d == 2 * HW
    ax, ay, ac = lax.axis_index("x"), lax.axis_index("y"), lax.axis_index("c")
    chip = 2 * ax + ay
    dev = 2 * chip + ac
    c_arr = jnp.reshape(ac, (1,)).astype(jnp.int32)
    chip_arr = jnp.reshape(chip, (1,)).astype(jnp.int32)
    transposed = ("w_in", "w_ff_gate", "w_ff_up")
    view = lambda a, nm: a[0].T if nm in transposed else a[0]

    sems_in, lands_in, token_in0 = _blocks_start([_cast_into_blocks(chip_arr, view(w_in, "w_in"), "cast_w_in")],
                                                 "gather_w_in_start")

    nc = d // 128
    pad8 = lambda a: jnp.pad(a, ((0, -a.shape[0] % 8), (0, 0)))
    small1 = jnp.concatenate([c.reshape(nc, 128) + token_in0[0, 0], pad8(hg_lb.reshape(4, 128)),
                              gla_w_gk.reshape(2 * RANK, 128), pad8(gla_b_gk.reshape(2, 128))], axis=0)
    blocks = [_cast_into_blocks(chip_arr, view(w_, nm), "cast_" + nm) for w_, nm in (
        (w_br_hg, "w_br_hg"), (w_br_gla, "w_br_gla"), (w_out, "w_out"), (w_ff_gate, "w_ff_gate"),
        (w_ff_up, "w_ff_up"), (w_ff_down, "w_ff_down"))]
    got1 = _allgather8(small1, "gather_small_params", after=blocks)
    c_all = got1[:, :nc, :].reshape(N_DEV, d)
    per_chip = got1[0::2]
    lb_full = per_chip[:, nc:nc + 4, :].transpose(1, 0, 2).reshape(4, HW)
    wgk_full = per_chip[:, nc + 8:nc + 8 + 2 * RANK, :].transpose(1, 0, 2).reshape(2, RANK, HW)
    bgk_full = per_chip[:, nc + 8 + 2 * RANK:nc + 10 + 2 * RANK, :].transpose(1, 0, 2).reshape(2, HW)
    wgk_pad = [jnp.zeros((128, HW), F32).at[dd * RANK:(dd + 1) * RANK].set(wgk_full[dd]) for dd in range(2)]
    bgk = [bgk_full[dd:dd + 1] for dd in range(2)]

    n_mod_cols = w_mod.shape[2]
    cond = jnp.concatenate([c_all, pad8(c_ctx.reshape(1, d))], axis=0)
    b_cols = lax.dynamic_slice(b_mod, (0, chip * n_mod_cols), (1, n_mod_cols))
    lands_in = _blocks_wait(sems_in, lands_in, [got1], "gather_w_in_wait")
    fwd_sems, lands_in, fwd_token = _forward_start(lands_in, "gather_w_in_forward_start")
    mod_part = _mod_forward(cond + fwd_token[0, 0], w_mod[0], b_cols, "mod_forward")
    mod_got = _allgather8(mod_part, "gather_mod")
    mod_all = mod_got[0::2].transpose(1, 0, 2).reshape(16, N_CHIP * n_mod_cols)
    modx = pad8(lax.dynamic_slice(mod_all, (dev, 0), (1, N_MOD * d)).reshape(N_MOD, d))
    modc = pad8(mod_all[8].reshape(N_MOD, d))

    gathered_in = _forward_wait(fwd_sems, lands_in, [mod_got], "gather_w_in_forward_wait")
    sems, lands, token = _blocks_start(blocks, "gather_rest_start", after=[gathered_in[0]])
    w_in_r = gathered_in[0].reshape(-1, d)

    norms = jnp.concatenate([norm_pre1, norm_post1, norm_pre2, norm_post2, jnp.zeros((4, d), F32)], axis=0)
    onorms = jnp.zeros((8, d), F32).at[0, :HD].set(hg_onorm[0]).at[1, :HD].set(gla_onorm[0])
    gla_side = [(wgk_pad[dd], bgk[dd]) for dd in range(2)]
    modx = modx + token[0, 0]
    front = _sample_front(x[0], ctx[0], modc, modx, norm_pre1, lb_full, gla_side, w_in_r)
    lands = _blocks_wait(sems, lands, front["o_list"], "gather_rest_wait")
    gathered = _blocks_finish(lands[:3], "gather_mix_finish")
    wbh, wbg = _unblocked(gathered[0]), _unblocked(gathered[1])
    wout = gathered[2].reshape(d, d)
    ffn_sems, ffn_lands, ffn_token = _forward_start(lands[3:], "gather_ffn_forward_start")
    onorms = onorms + ffn_token[0, 0]

    def ffn_weights(after):
        got = _forward_wait(ffn_sems, ffn_lands, after, "gather_ffn_forward_wait")
        return tuple(g.reshape(-1, d) for g in got)

    dff = w_ff_down.shape[1] * N_CHIP
    groups = {"ffn": ["w_ff_gate", "w_ff_up", "w_ff_down"], "mix": ["w_br_hg", "w_br_gla", "w_out"], "in": ["w_in"]}
    row_sharded = {"w_out": d // N_CHIP, "w_ff_down": dff // N_CHIP, "w_ff_gate": dff // N_CHIP,
                   "w_ff_up": dff // N_CHIP, "w_in": w_in.shape[2]}
    in_flight, to_sibling, small = {}, {}, {}

    def reduce_small(stats):
        small2 = jnp.concatenate([
            stats["stat_in"], stats["stat_mix"], stats["stat_ffn"],
            jnp.concatenate(stats["dlb"], axis=1), jnp.concatenate(stats["dbgk"], axis=1),
            jnp.concatenate([stats["dwgk"][0][0:RANK], stats["dwgk"][1][RANK:2 * RANK]], axis=1)], axis=0)
        assert small2.shape[0] == SMALL_ROWS
        got2 = _allgather8(small2, "gather_small_grads")
        total, dmod_all, g_b_mod, g_lb_full = _reduce_small(got2, lb_full, "reduce_small")
        dmod_cols = lax.dynamic_slice(dmod_all, (0, chip * n_mod_cols), (16, n_mod_cols))
        g_w_mod, cctx_part = _mod_backward(cond, w_mod[0], dmod_cols, "mod_backward")
        got3 = _allgather8(cctx_part, "gather_c_ctx_grad")
        g_c_ctx = _c_ctx_grad(got3, c_ctx.reshape(1, d), "c_ctx_grad")
        small.update(total=total, g_b_mod=g_b_mod, g_lb_full=g_lb_full, g_w_mod=g_w_mod, g_c_ctx=g_c_ctx)

    def reduce(group, grads):
        if group == "small":
            return reduce_small(grads)
        if group.startswith("push_"):
            return push(group[5:], grads)
        nms = groups[group]
        full = [g.reshape(N_CHIP, row_sharded[nm], d) if nm in row_sharded else _blocked(g, N_CHIP)
                for g, nm in zip(grads, nms)]
        sems_, full, lands_, token_ = _send_half_start(full, "grads_to_sibling_start_" + group)
        to_sibling[group] = (sems_, full, lands_)
        return token_[0, 0]

    def push(group, after):
        nms = groups[group]
        sems_, full, lands_ = to_sibling[group]
        if group == "in":
            after = list(after) + [small["g_c_ctx"], small["total"]]
        full, from_sibling = _send_half_wait(sems_, full, lands_, after, "grads_to_sibling_wait_" + group)
        pairs = [_pair_sum(c_arr, f, r_, "pair_sum_" + nm) for f, r_, nm in zip(full, from_sibling, nms)]
        after = [small["g_c_ctx"], small["total"]] if group == "in" else []
        sems_, pairs, lands_, token_ = _scatter_start(pairs, "grads_to_owner_start_" + group, after)
        in_flight[group] = (sems_, pairs, lands_, token_)
        return token_[0, 0]

    r = _sample_back(reduce, front, x[0], ctx[0], loss_target[0], modc, modx, norm_pre1, norms, onorms, lb_full,
                     gla_side, w_in_r, wbh, wbg, wout, ffn_weights)
    grad_x = r["grad_x"]

    weights = dict(w_in=(w_in, m_w_in, v_w_in), w_br_hg=(w_br_hg, m_w_br_hg, v_w_br_hg),
                   w_br_gla=(w_br_gla, m_w_br_gla, v_w_br_gla), w_out=(w_out, m_w_out, v_w_out),
                   w_ff_gate=(w_ff_gate, m_w_ff_gate, v_w_ff_gate), w_ff_up=(w_ff_up, m_w_ff_up, v_w_ff_up),
                   w_ff_down=(w_ff_down, m_w_ff_down, v_w_ff_down))
    names = ["w_in", "w_br_hg", "w_br_gla", "w_out", "w_ff_gate", "w_ff_up", "w_ff_down"]
    big, swapping = {}, {}

    def sum_and_swap(group, after):
        sems_, pairs, lands_, _ = in_flight[group]
        pairs, lands_ = _scatter_wait(sems_, pairs, lands_, after, "grads_to_owner_wait_" + group)
        own_half = [_sum_owner(chip_arr, pr, g, "chip_sum_" + nm) for pr, g, nm in zip(pairs, lands_, groups[group])]
        swapping[group] = _swap_start(own_half, "halves_to_sibling_start_" + group)
        return own_half[-1]

    def update(group, after):
        sems_, own_half, lands_ = swapping[group]
        own_half, other_half = _swap_wait(sems_, own_half, lands_, after, "halves_to_sibling_wait_" + group)
        done = []
        for nm, own, oth in zip(groups[group], own_half, other_half):
            w_, m_, v_ = (view(a, nm) for a in weights[nm])
            res = _adamw_halves(c_arr, own, oth, w_, m_, v_, "adamw_" + nm)
            big[nm] = [r_.T[None] if nm in transposed else r_[None] for r_ in res]
            done.append(res[1])
        return done

    token_in = in_flight["in"][3]
    summed_ffn = sum_and_swap("ffn", [token_in])
    summed_mix = sum_and_swap("mix", [summed_ffn])

    total, g_b_mod, g_lb_full, g_w_mod, g_c_ctx = (small[k] for k in ("total", "g_b_mod", "g_lb_full", "g_w_mod",
                                                                      "g_c_ctx"))
    g_pre1, g_post1, g_pre2, g_post2 = (total[r_:r_ + 1] for r_ in (ROW_PRE1, ROW_POST1, ROW_PRE2, ROW_POST2))
    g_hg_on, g_gla_on = total[ROW_ONORM:ROW_ONORM + 1, 0:HD], total[ROW_ONORM:ROW_ONORM + 1, HD:2 * HD]
    n_lb = hg_lb.shape[2]
    g_hg_lb = lax.dynamic_slice(g_lb_full, (0, chip * n_lb), (4, n_lb))
    g_bgk = lax.dynamic_slice(total[ROW_BGK:ROW_BGK + 1].reshape(2, HW), (0, chip * n_lb), (2, n_lb))
    g_wgk_full = total[ROW_WGK:ROW_WGK + RANK].reshape(RANK, 2, HW).transpose(1, 0, 2).reshape(2 * RANK, HW)
    g_wgk = lax.dynamic_slice(g_wgk_full, (0, chip * n_lb), (2 * RANK, n_lb))

    small_items = [
        (g_c_ctx, c_ctx.reshape(1, d), m_c_ctx.reshape(1, d), v_c_ctx.reshape(1, d)),
        (g_b_mod, b_mod, m_b_mod, v_b_mod),
        (g_pre1, norm_pre1, m_norm_pre1, v_norm_pre1),
        (g_post1, norm_post1, m_norm_post1, v_norm_post1),
        (g_pre2, norm_pre2, m_norm_pre2, v_norm_pre2),
        (g_post2, norm_post2, m_norm_post2, v_norm_post2),
        (g_hg_lb, hg_lb.reshape(4, n_lb), m_hg_lb.reshape(4, n_lb), v_hg_lb.reshape(4, n_lb)),
        (g_hg_on, hg_onorm, m_hg_onorm, v_hg_onorm),
        (g_wgk, gla_w_gk.reshape(2 * RANK, n_lb), m_gla_w_gk.reshape(2 * RANK, n_lb), v_gla_w_gk.reshape(2 * RANK, n_lb)),
        (g_bgk, gla_b_gk.reshape(2, n_lb), m_gla_b_gk.reshape(2, n_lb), v_gla_b_gk.reshape(2, n_lb)),
        (g_gla_on, gla_onorm, m_gla_onorm, v_gla_onorm),
    ]
    small_res = _adamw_whole(small_items, "adamw_small")
    mod_res = _adamw_tiled(g_w_mod, w_mod[0], m_w_mod[0], v_w_mod[0], "adamw_w_mod")
    done_ffn = update("ffn", [summed_mix, mod_res[0], small_res[0][0]])
    done_mix = update("mix", done_ffn)
    update("in", [sum_and_swap("in", done_mix)])

    loss = total[ROW_LOSS, 0]

    shapes = dict(c_ctx=c_ctx.shape, b_mod=b_mod.shape, norm_pre1=norm_pre1.shape, norm_post1=norm_post1.shape,
                  norm_pre2=norm_pre2.shape, norm_post2=norm_post2.shape, hg_lb=hg_lb.shape, hg_onorm=hg_onorm.shape,
                  gla_w_gk=gla_w_gk.shape, gla_b_gk=gla_b_gk.shape, gla_onorm=gla_onorm.shape)
    small_names = ["c_ctx", "b_mod", "norm_pre1", "norm_post1", "norm_pre2", "norm_post2", "hg_lb", "hg_onorm",
                   "gla_w_gk", "gla_b_gk", "gla_onorm"]
    grads, deltas, new_m, new_v = {}, {}, {}, {}
    for nm, item, res in zip(small_names, small_items, small_res):
        grads[nm] = item[0].reshape(shapes[nm])
        deltas[nm], new_m[nm], new_v[nm] = (r_.reshape(shapes[nm]) for r_ in res)
    grads["w_mod"] = g_w_mod[None]
    deltas["w_mod"], new_m["w_mod"], new_v["w_mod"] = (r_[None] for r_ in mod_res)
    for nm in names:
        grads[nm], deltas[nm], new_m[nm], new_v[nm] = big[nm]
    order = ["c_ctx", "w_mod", "b_mod", "norm_pre1", "norm_post1", "norm_pre2", "norm_post2", "w_in", "hg_lb",
             "hg_onorm", "gla_w_gk", "gla_b_gk", "gla_onorm", "w_br_hg", "w_br_gla", "w_out", "w_ff_gate", "w_ff_up",
             "w_ff_down"]
    return (loss, grad_x[None], *[grads[n] for n in order], *[deltas[n] for n in order],
            *[new_m[n] for n in order], *[new_v[n] for n in order])
```

```python
import functools

import jax
import jax.numpy as jnp
from jax import lax
from jax.experimental import pallas as pl
from jax.experimental.pallas import tpu as pltpu

F32 = jnp.float32
BF16 = jnp.bfloat16
MESH = pl.DeviceIdType.MESH

EPS = 1e-6
CHUNK = 64
SUB = 16
NSUB = CHUNK // SUB
NH = 4
HD = 128
HW = NH * HD
RANK = 16
GATE_NORM = 16.0
N_MOD = 6
TM = 256
BWD_LANES = 8
N_DEV = 8
N_CHIP = 4
VMEM_LIMIT = 56 * 1024 * 1024

ADAM_LR = 0.001
ADAM_B1 = 0.9
ADAM_B2 = 0.999
ADAM_EPS = 1e-08
ADAM_WD = 0.01
ADAM_STEP = 10

VMEM_SPEC = pl.BlockSpec(memory_space=pltpu.VMEM)
ANY_SPEC = pl.BlockSpec(memory_space=pl.ANY)
HBM_SPEC = pl.BlockSpec(memory_space=pltpu.HBM)
SEM_SPEC = pl.BlockSpec(memory_space=pltpu.SEMAPHORE)
EFFECT = pltpu.SideEffectType.DATAFLOW_SIDE_EFFECTING


def _cparams(**kw):
    return pltpu.CompilerParams(vmem_limit_bytes=VMEM_LIMIT, **kw)


def _dot(a, b):
    return jnp.dot(a.astype(BF16), b.astype(BF16), preferred_element_type=F32)


def _dot_nt(a, b):
    return lax.dot_general(a.astype(BF16), b.astype(BF16), (((1,), (1,)), ((), ())), preferred_element_type=F32)


def _dot_tn(a, b):
    return lax.dot_general(a.astype(BF16), b.astype(BF16), (((0,), (0,)), ((), ())), preferred_element_type=F32)


def _sigmoid(x):
    return 1.0 / (1.0 + jnp.exp(-x))


def _silu(x):
    return x * _sigmoid(x)


def _dsilu(x):
    s = _sigmoid(x)
    return s * (1.0 + x * (1.0 - s))


def _log_sigmoid(x):
    return jnp.minimum(x, 0.0) - jnp.log(1.0 + jnp.exp(-jnp.abs(x)))


def _colsum(a):
    return jnp.sum(a, axis=0, keepdims=True)


def _rms(a):
    r = lax.rsqrt(jnp.mean(a * a, axis=-1, keepdims=True) + EPS)
    return a * r, r


def _rms_bwd(dn, n, r):
    return r * (dn - n * jnp.mean(dn * n, axis=-1, keepdims=True))


def _place():
    x, y, c = lax.axis_index("x"), lax.axis_index("y"), lax.axis_index("c")
    chips = [(1 - x, y), (x, 1 - y), (1 - x, 1 - y)]
    return x, y, c, chips


def _allgather8(v, name, after=()):
    rows, cols = v.shape
    n_after = len(after)

    def body(x_ref, *rest):
        out_ref, send_sems, recv_sems, local_sem = rest[n_after:]
        x, y, c, chips = _place()
        me, sibling = (x, y, c), (x, y, 1 - c)

        def blk(px, py, pc):
            return out_ref.at[4 * px + 2 * py + pc]

        def copy(k, block, to, src=None):
            return pltpu.make_async_remote_copy(
                src_ref=blk(*block) if src is None else src, dst_ref=blk(*block),
                send_sem=send_sems.at[k], recv_sem=recv_sems.at[k], device_id=to, device_id_type=MESH)

        mine = pltpu.make_async_copy(x_ref, blk(*me), local_sem)
        mine.start()
        first = [copy(0, me, sibling, src=x_ref)]
        first += [copy(1 + j, me, (*chip, c), src=x_ref) for j, chip in enumerate(chips)]
        for cp in first:
            cp.start()
        passed = [copy(4 + j, (*chip, c), sibling) for j, chip in enumerate(chips)]
        for j, chip in enumerate(chips):
            copy(1 + j, (*chip, c), me).wait_recv()
            passed[j].start()
        copy(0, sibling, me).wait_recv()
        for j, chip in enumerate(chips):
            copy(4 + j, (*chip, 1 - c), me).wait_recv()
        for cp in first + passed:
            cp.wait_send()
        mine.wait()

    return pl.pallas_call(
        body, name=name,
        out_shape=jax.ShapeDtypeStruct((N_DEV, rows, cols), v.dtype),
        in_specs=[VMEM_SPEC] + [ANY_SPEC] * n_after, out_specs=VMEM_SPEC,
        scratch_shapes=[pltpu.SemaphoreType.DMA((7,)), pltpu.SemaphoreType.DMA((7,)), pltpu.SemaphoreType.DMA],
    )(v, *after)


def _allgather8_start(v, name):
    rows, cols = v.shape

    def body(x_ref, out_ref, *rest):
        send_sems, recv_sems, token, local_sem = rest[:4], rest[4:8], rest[-2], rest[-1]
        x, y, c, chips = _place()
        own = out_ref.at[4 * x + 2 * y + c]
        mine = pltpu.make_async_copy(x_ref, own, local_sem)
        mine.start()
        mine.wait()
        for k, to in enumerate([(x, y, 1 - c)] + [(*chip, c) for chip in chips]):
            pltpu.make_async_remote_copy(src_ref=own, dst_ref=own, send_sem=send_sems[k], recv_sem=recv_sems[k],
                                         device_id=to, device_id_type=MESH).start()
        token[...] = jnp.zeros_like(token)

    land = _hbm(lax.empty((N_DEV, rows, cols), v.dtype))
    out = pl.pallas_call(
        body, name=name,
        out_shape=(*[pltpu.SemaphoreType.DMA(())] * 8, pltpu.HBM(land.shape, land.dtype),
                   jax.ShapeDtypeStruct((8, 128), F32)),
        in_specs=[VMEM_SPEC, HBM_SPEC],
        out_specs=(*[SEM_SPEC] * 8, HBM_SPEC, VMEM_SPEC),
        input_output_aliases={1: 8},
        scratch_shapes=[pltpu.SemaphoreType.DMA],
        compiler_params=pltpu.CompilerParams(has_side_effects=EFFECT),
    )(v, land)
    return list(out[:8]), out[8], out[9]


def _allgather8_wait(sems, land, after, name):
    def blk(ref, px, py, pc):
        return ref.at[4 * px + 2 * py + pc]

    def wait_body(out_ref, *rest):
        send_sems, recv_sems = rest[:4], rest[4:8]
        x, y, c, chips = _place()
        me = (x, y, c)
        for k, peer in enumerate([(x, y, 1 - c)] + [(*chip, c) for chip in chips]):
            sent = pltpu.make_async_remote_copy(
                src_ref=blk(out_ref, *me), dst_ref=blk(out_ref, *me), send_sem=send_sems[k], recv_sem=recv_sems[k],
                device_id=peer, device_id_type=MESH)
            sent.wait_send()
            pltpu.make_async_remote_copy(
                src_ref=blk(out_ref, *peer), dst_ref=blk(out_ref, *peer), send_sem=send_sems[k],
                recv_sem=recv_sems[k], device_id=me, device_id_type=MESH).wait_recv()

    def pass_body(out_ref, _, send_sems, recv_sems):
        x, y, c, chips = _place()
        started = []
        for j, chip in enumerate(chips):
            cp = pltpu.make_async_remote_copy(
                src_ref=blk(out_ref, *chip, c), dst_ref=blk(out_ref, *chip, c), send_sem=send_sems.at[j],
                recv_sem=recv_sems.at[j], device_id=(x, y, 1 - c), device_id_type=MESH)
            cp.start()
            started.append(cp)
        for j, chip in enumerate(chips):
            pltpu.make_async_remote_copy(
                src_ref=blk(out_ref, *chip, 1 - c), dst_ref=blk(out_ref, *chip, 1 - c), send_sem=send_sems.at[j],
                recv_sem=recv_sems.at[j], device_id=(x, y, c), device_id_type=MESH).wait_recv()
        for cp in started:
            cp.wait_send()

    land = pl.pallas_call(
        wait_body, name=name,
        out_shape=pltpu.HBM(land.shape, land.dtype),
        in_specs=[HBM_SPEC] + [SEM_SPEC] * 8 + [ANY_SPEC] * len(after),
        out_specs=HBM_SPEC,
        input_output_aliases={0: 0},
        compiler_params=pltpu.CompilerParams(has_side_effects=EFFECT),
    )(land, *sems, *after)
    return pl.pallas_call(
        pass_body, name=name + "_pass",
        out_shape=jax.ShapeDtypeStruct(land.shape, land.dtype),
        in_specs=[ANY_SPEC], out_specs=ANY_SPEC,
        input_output_aliases={0: 0},
        scratch_shapes=[pltpu.SemaphoreType.DMA((3,)), pltpu.SemaphoreType.DMA((3,))],
    )(land)


def _cast_into_blocks(chip_arr, w, name):
    rows, cols = w.shape
    tr = _row_tile(rows, 16, 256)

    def body(chip_ref, w_ref, o_ref):
        o_ref[0] = w_ref[...].astype(BF16)

    return pl.pallas_call(
        body, name=name,
        grid_spec=pltpu.PrefetchScalarGridSpec(
            num_scalar_prefetch=1, grid=(rows // tr,),
            in_specs=[pl.BlockSpec((tr, cols), lambda i, chip_ref: (i, 0))],
            out_specs=pl.BlockSpec((1, tr, cols), lambda i, chip_ref: (chip_ref[0], i, 0))),
        out_shape=jax.ShapeDtypeStruct((N_CHIP, rows, cols), BF16),
        compiler_params=_cparams(dimension_semantics=("parallel",)),
    )(chip_arr, w)


def _halved_by_rows(shape):
    return (shape[1] // 2) % 16 == 0


def _half_of(ref, pc, block=None):
    lead = slice(None) if block is None else block
    if _halved_by_rows(ref.shape):
        h = ref.shape[1] // 2
        return ref.at[lead, pl.ds(pl.multiple_of(pc * h, 16), h), :]
    h = ref.shape[2] // 2
    return ref.at[lead, :, pl.ds(pl.multiple_of(pc * h, 128), h)]


def _half_shape(shape):
    return (shape[0], shape[1] // 2, shape[2]) if _halved_by_rows(shape) else (shape[0], shape[1], shape[2] // 2)


def _half_rows(ref, chip_id, pc):
    return _half_of(ref, pc, chip_id)


def _hbm(a):
    return pltpu.with_memory_space_constraint(a, pltpu.HBM)


def _blocks_start(lands, name, after=()):
    n = len(lands)
    n_sem = 3 * n
    first = n + len(after)

    def body(*refs):
        lnd = refs[:n]
        send_sems, recv_sems = refs[first:first + n_sem], refs[first + n_sem:first + 2 * n_sem]
        token = refs[-1]
        x, y, c, chips = _place()
        me_chip = 2 * x + y
        for k in range(n):
            for j, chip in enumerate(chips):
                pltpu.make_async_remote_copy(
                    src_ref=_half_rows(lnd[k], me_chip, c), dst_ref=_half_rows(lnd[k], me_chip, c),
                    send_sem=send_sems[3 * k + j], recv_sem=recv_sems[3 * k + j],
                    device_id=(*chip, c), device_id_type=MESH).start()
        token[...] = jnp.zeros_like(token)

    out = pl.pallas_call(
        body, name=name,
        out_shape=(*[pltpu.SemaphoreType.DMA(())] * (2 * n_sem),
                   *[pltpu.HBM(l.shape, l.dtype) for l in lands],
                   jax.ShapeDtypeStruct((8, 128), F32)),
        in_specs=[HBM_SPEC] * n + [ANY_SPEC] * len(after),
        out_specs=(*[SEM_SPEC] * (2 * n_sem), *[HBM_SPEC] * n, VMEM_SPEC),
        input_output_aliases={i: 2 * n_sem + i for i in range(n)},
        compiler_params=pltpu.CompilerParams(has_side_effects=EFFECT),
    )(*[_hbm(l) for l in lands], *after)
    return list(out[:2 * n_sem]), list(out[2 * n_sem:2 * n_sem + n]), out[-1]


def _blocks_wait(sems, lands, after, name):
    n = len(lands)
    n_sem = 3 * n

    def body(*refs):
        lnd = refs[:n]
        s_sems, r_sems = refs[n:n + n_sem], refs[n + n_sem:n + 2 * n_sem]
        x, y, c, chips = _place()
        me_chip = 2 * x + y
        for k in range(n):
            for j, (px, py) in enumerate(chips):
                cp = pltpu.make_async_remote_copy(
                    src_ref=_half_rows(lnd[k], me_chip, c), dst_ref=_half_rows(lnd[k], 2 * px + py, c),
                    send_sem=s_sems[3 * k + j], recv_sem=r_sems[3 * k + j],
                    device_id=(px, py, c), device_id_type=MESH)
                cp.wait_send()
                cp.wait_recv()

    out = pl.pallas_call(
        body, name=name,
        out_shape=tuple(pltpu.HBM(l.shape, l.dtype) for l in lands),
        in_specs=[HBM_SPEC] * n + [SEM_SPEC] * (2 * n_sem) + [ANY_SPEC] * len(after),
        out_specs=[HBM_SPEC] * n,
        input_output_aliases={i: i for i in range(n)},
        compiler_params=pltpu.CompilerParams(has_side_effects=EFFECT),
    )(*lands, *sems, *after)
    return list(out)


def _forward_start(lands, name):
    n = len(lands)
    n_sem = 3 * n

    def body(*refs):
        lnd = refs[:n]
        send_sems, recv_sems = refs[n:n + n_sem], refs[n + n_sem:n + 2 * n_sem]
        x, y, c, chips = _place()
        for k in range(n):
            for j, (px, py) in enumerate(chips):
                pltpu.make_async_remote_copy(
                    src_ref=_half_rows(lnd[k], 2 * px + py, c), dst_ref=_half_rows(lnd[k], 2 * px + py, c),
                    send_sem=send_sems[3 * k + j], recv_sem=recv_sems[3 * k + j],
                    device_id=(x, y, 1 - c), device_id_type=MESH).start()
        refs[-1][...] = jnp.zeros_like(refs[-1])

    out = pl.pallas_call(
        body, name=name,
        out_shape=(*[pltpu.SemaphoreType.DMA(())] * (2 * n_sem), *[pltpu.HBM(l.shape, l.dtype) for l in lands],
                   jax.ShapeDtypeStruct((8, 128), F32)),
        in_specs=[HBM_SPEC] * n,
        out_specs=(*[SEM_SPEC] * (2 * n_sem), *[HBM_SPEC] * n, VMEM_SPEC),
        input_output_aliases={i: 2 * n_sem + i for i in range(n)},
        compiler_params=pltpu.CompilerParams(has_side_effects=EFFECT),
    )(*[_hbm(l) for l in lands])
    return list(out[:2 * n_sem]), list(out[2 * n_sem:2 * n_sem + n]), out[-1]


def _forward_wait(sems, lands, after, name):
    n = len(lands)
    n_sem = 3 * n

    def body(*refs):
        lnd = refs[:n]
        s_sems, r_sems = refs[n:n + n_sem], refs[n + n_sem:n + 2 * n_sem]
        x, y, c, chips = _place()
        for k in range(n):
            for j, (px, py) in enumerate(chips):
                cp = pltpu.make_async_remote_copy(
                    src_ref=_half_rows(lnd[k], 2 * px + py, c), dst_ref=_half_rows(lnd[k], 2 * px + py, 1 - c),
                    send_sem=s_sems[3 * k + j], recv_sem=r_sems[3 * k + j],
                    device_id=(x, y, 1 - c), device_id_type=MESH)
                cp.wait_send()
                cp.wait_recv()

    out = pl.pallas_call(
        body, name=name,
        out_shape=tuple(pltpu.HBM(l.shape, l.dtype) for l in lands),
        in_specs=[HBM_SPEC] * n + [SEM_SPEC] * (2 * n_sem) + [ANY_SPEC] * len(after),
        out_specs=[HBM_SPEC] * n,
        input_output_aliases={i: i for i in range(n)},
        compiler_params=pltpu.CompilerParams(has_side_effects=EFFECT),
    )(*lands, *sems, *after)
    return list(out)


def _blocks_finish(lands, name):
    n = len(lands)

    def body(*refs):
        lnd = refs[n:2 * n]
        send_sems, recv_sems = refs[2 * n:]
        x, y, c, chips = _place()
        sibling = (x, y, 1 - c)

        def copy(k, j, chip_id, pc):
            return pltpu.make_async_remote_copy(
                src_ref=_half_rows(lnd[k], chip_id, pc), dst_ref=_half_rows(lnd[k], chip_id, pc),
                send_sem=send_sems.at[k, j], recv_sem=recv_sems.at[k, j], device_id=sibling, device_id_type=MESH)

        started = []
        for k in range(n):
            for j, (px, py) in enumerate(chips):
                cp = copy(k, j, 2 * px + py, c)
                cp.start()
                started.append(cp)
        for k in range(n):
            for j, (px, py) in enumerate(chips):
                copy(k, j, 2 * px + py, 1 - c).wait_recv()
        for cp in started:
            cp.wait_send()

    out = pl.pallas_call(
        body, name=name,
        out_shape=[jax.ShapeDtypeStruct(l.shape, l.dtype) for l in lands],
        in_specs=[ANY_SPEC] * n, out_specs=[ANY_SPEC] * n,
        input_output_aliases={i: i for i in range(n)},
        scratch_shapes=[pltpu.SemaphoreType.DMA((n, 3)), pltpu.SemaphoreType.DMA((n, 3))],
    )(*lands)
    return list(out)


def _send_half_start(arrs, name):
    n = len(arrs)

    def body(*refs):
        ins, lnd = refs[:n], refs[n:2 * n]
        send_sems, recv_sems = refs[2 * n:3 * n], refs[3 * n:4 * n]
        token = refs[-1]
        x, y, c, _ = _place()
        for k in range(n):
            pltpu.make_async_remote_copy(
                src_ref=_half_of(ins[k], 1 - c), dst_ref=lnd[k], send_sem=send_sems[k], recv_sem=recv_sems[k],
                device_id=(x, y, 1 - c), device_id_type=MESH).start()
        token[...] = jnp.zeros_like(token)

    lands = [_hbm(lax.empty(_half_shape(a.shape), a.dtype)) for a in arrs]
    out = pl.pallas_call(
        body, name=name,
        out_shape=(*[pltpu.SemaphoreType.DMA(())] * (2 * n), *[pltpu.HBM(a.shape, a.dtype) for a in arrs],
                   *[pltpu.HBM(l.shape, l.dtype) for l in lands], jax.ShapeDtypeStruct((8, 128), F32)),
        in_specs=[HBM_SPEC] * (2 * n),
        out_specs=(*[SEM_SPEC] * (2 * n), *[HBM_SPEC] * (2 * n), VMEM_SPEC),
        input_output_aliases={i: 2 * n + i for i in range(2 * n)},
        compiler_params=pltpu.CompilerParams(has_side_effects=EFFECT),
    )(*[_hbm(a) for a in arrs], *lands)
    return list(out[:2 * n]), list(out[2 * n:3 * n]), list(out[3 * n:4 * n]), out[-1]


def _send_half_wait(sems, arrs, lands, after, name):
    n = len(arrs)

    def body(*refs):
        ins, lnd = refs[:n], refs[n:2 * n]
        s_sems, r_sems = refs[2 * n:3 * n], refs[3 * n:4 * n]
        x, y, c, _ = _place()
        for k in range(n):
            cp = pltpu.make_async_remote_copy(
                src_ref=_half_of(ins[k], 1 - c), dst_ref=lnd[k], send_sem=s_sems[k], recv_sem=r_sems[k],
                device_id=(x, y, 1 - c), device_id_type=MESH)
            cp.wait_send()
            cp.wait_recv()

    out = pl.pallas_call(
        body, name=name,
        out_shape=tuple(pltpu.HBM(a.shape, a.dtype) for a in list(arrs) + list(lands)),
        in_specs=[HBM_SPEC] * (2 * n) + [SEM_SPEC] * (2 * n) + [ANY_SPEC] * len(after),
        out_specs=[HBM_SPEC] * (2 * n),
        input_output_aliases={i: i for i in range(2 * n)},
        compiler_params=pltpu.CompilerParams(has_side_effects=EFFECT),
    )(*arrs, *lands, *sems, *after)
    return list(out[:n]), list(out[n:])


def _scatter_start(arrs, name, after=()):
    n = len(arrs)
    n_sem = 3 * n
    first = 2 * n + len(after)

    def body(*refs):
        ins, lnd = refs[:n], refs[n:2 * n]
        send_sems, recv_sems = refs[first:first + n_sem], refs[first + n_sem:first + 2 * n_sem]
        token = refs[-1]
        x, y, c, chips = _place()
        me_chip = 2 * x + y
        for k in range(n):
            for j, (px, py) in enumerate(chips):
                pltpu.make_async_remote_copy(
                    src_ref=ins[k].at[2 * px + py], dst_ref=lnd[k].at[me_chip],
                    send_sem=send_sems[3 * k + j], recv_sem=recv_sems[3 * k + j],
                    device_id=(px, py, c), device_id_type=MESH).start()
        token[...] = jnp.zeros_like(token)

    lands = [_hbm(lax.empty(a.shape, a.dtype)) for a in arrs]
    out = pl.pallas_call(
        body, name=name,
        out_shape=(*[pltpu.SemaphoreType.DMA(())] * (2 * n_sem),
                   *[pltpu.HBM(a.shape, a.dtype) for a in arrs], *[pltpu.HBM(a.shape, a.dtype) for a in arrs],
                   jax.ShapeDtypeStruct((8, 128), F32)),
        in_specs=[HBM_SPEC] * (2 * n) + [ANY_SPEC] * len(after),
        out_specs=(*[SEM_SPEC] * (2 * n_sem), *[HBM_SPEC] * (2 * n), VMEM_SPEC),
        input_output_aliases={i: 2 * n_sem + i for i in range(2 * n)},
        compiler_params=pltpu.CompilerParams(has_side_effects=EFFECT),
    )(*[_hbm(a) for a in arrs], *lands, *after)
    base = 2 * n_sem
    return list(out[:base]), list(out[base:base + n]), list(out[base + n:base + 2 * n]), out[-1]


def _scatter_wait(sems, arrs, lands, after, name):
    n = len(arrs)
    n_sem = 3 * n

    def body(*refs):
        ins, lnd = refs[:n], refs[n:2 * n]
        s_sems, r_sems = refs[2 * n:2 * n + n_sem], refs[2 * n + n_sem:2 * n + 2 * n_sem]
        x, y, c, chips = _place()
        for k in range(n):
            for j, (px, py) in enumerate(chips):
                cp = pltpu.make_async_remote_copy(
                    src_ref=ins[k].at[2 * px + py], dst_ref=lnd[k].at[2 * px + py],
                    send_sem=s_sems[3 * k + j], recv_sem=r_sems[3 * k + j],
                    device_id=(px, py, c), device_id_type=MESH)
                cp.wait_send()
                cp.wait_recv()

    out = pl.pallas_call(
        body, name=name,
        out_shape=tuple(pltpu.HBM(a.shape, a.dtype) for a in list(arrs) + list(lands)),
        in_specs=[HBM_SPEC] * (2 * n) + [SEM_SPEC] * (2 * n_sem) + [ANY_SPEC] * len(after),
        out_specs=[HBM_SPEC] * (2 * n),
        input_output_aliases={i: i for i in range(2 * n)},
        compiler_params=pltpu.CompilerParams(has_side_effects=EFFECT),
    )(*arrs, *lands, *sems, *after)
    return list(out[:n]), list(out[n:])


def _sum_owner(chip_arr, pairs, got, name):
    nb, h, cols = got.shape
    tr = _row_tile(h, 16, 256)

    def body(chip_ref, own_ref, a_ref, b_ref, c_ref, o_ref):
        o_ref[...] = ((own_ref[0].astype(F32) + a_ref[0].astype(F32)) + b_ref[0].astype(F32)) + c_ref[0].astype(F32)

    def slot(off):
        return pl.BlockSpec((1, tr, cols), lambda i, chip_ref: ((chip_ref[0] + off) % N_CHIP, i, 0))

    return pl.pallas_call(
        body, name=name,
        grid_spec=pltpu.PrefetchScalarGridSpec(
            num_scalar_prefetch=1, grid=(h // tr,),
            in_specs=[slot(0), slot(1), slot(2), slot(3)],
            out_specs=pl.BlockSpec((tr, cols), lambda i, chip_ref: (i, 0))),
        out_shape=jax.ShapeDtypeStruct((h, cols), F32),
        compiler_params=_cparams(dimension_semantics=("parallel",)),
    )(chip_arr, pairs, got, got, got)


def _swap_start(arrs, name, after=()):
    n = len(arrs)
    first = 2 * n + len(after)

    def body(*refs):
        ins, lnd = refs[:n], refs[n:2 * n]
        send_sems, recv_sems = refs[first:first + n], refs[first + n:first + 2 * n]
        x, y, c, _ = _place()
        for k in range(n):
            pltpu.make_async_remote_copy(
                src_ref=ins[k], dst_ref=lnd[k], send_sem=send_sems[k], recv_sem=recv_sems[k],
                device_id=(x, y, 1 - c), device_id_type=MESH).start()

    lands = [_hbm(lax.empty(a.shape, a.dtype)) for a in arrs]
    out = pl.pallas_call(
        body, name=name,
        out_shape=(*[pltpu.SemaphoreType.DMA(())] * (2 * n), *[pltpu.HBM(a.shape, a.dtype) for a in arrs],
                   *[pltpu.HBM(a.shape, a.dtype) for a in arrs]),
        in_specs=[HBM_SPEC] * (2 * n) + [ANY_SPEC] * len(after),
        out_specs=(*[SEM_SPEC] * (2 * n), *[HBM_SPEC] * (2 * n)),
        input_output_aliases={i: 2 * n + i for i in range(2 * n)},
        compiler_params=pltpu.CompilerParams(has_side_effects=EFFECT),
    )(*[_hbm(a) for a in arrs], *lands, *after)
    return list(out[:2 * n]), list(out[2 * n:3 * n]), list(out[3 * n:4 * n])


def _swap_wait(sems, arrs, lands, after, name):
    n = len(arrs)

    def body(*refs):
        ins, lnd = refs[:n], refs[n:2 * n]
        s_sems, r_sems = refs[2 * n:3 * n], refs[3 * n:4 * n]
        x, y, c, _ = _place()
        for k in range(n):
            cp = pltpu.make_async_remote_copy(
                src_ref=ins[k], dst_ref=lnd[k], send_sem=s_sems[k], recv_sem=r_sems[k],
                device_id=(x, y, 1 - c), device_id_type=MESH)
            cp.wait_send()
            cp.wait_recv()

    out = pl.pallas_call(
        body, name=name,
        out_shape=tuple(pltpu.HBM(a.shape, a.dtype) for a in list(arrs) + list(lands)),
        in_specs=[HBM_SPEC] * (2 * n) + [SEM_SPEC] * (2 * n) + [ANY_SPEC] * len(after),
        out_specs=[HBM_SPEC] * (2 * n),
        input_output_aliases={i: i for i in range(2 * n)},
        compiler_params=pltpu.CompilerParams(has_side_effects=EFFECT),
    )(*arrs, *lands, *sems, *after)
    return list(out[:n]), list(out[n:])


def _row_tile(h, mult=8, cap=128):
    for t in range(cap - cap % mult, mult - 1, -mult):
        if h % t == 0:
            return t
    if mult > 8:
        return _row_tile(h, 8, cap)
    raise ValueError(h)


def _pair_sum(c_arr, full, recv, name):
    nb, rows, cols = full.shape

    def body(c_ref, f_ref, r_ref, o_ref):
        o_ref[...] = (f_ref[...] + r_ref[...]).astype(BF16)

    if _halved_by_rows(full.shape):
        h = rows // 2
        tr = _row_tile(h, 16, 256)
        steps = h // tr
        own = pl.BlockSpec((1, tr, cols), lambda b, i, c_ref: (b, c_ref[0] * steps + i, 0))
        half = pl.BlockSpec((1, tr, cols), lambda b, i, c_ref: (b, i, 0))
    else:
        steps = 1
        own = pl.BlockSpec((1, rows, cols // 2), lambda b, i, c_ref: (b, 0, c_ref[0]))
        half = pl.BlockSpec((1, rows, cols // 2), lambda b, i, c_ref: (b, 0, 0))
    return pl.pallas_call(
        body, name=name,
        grid_spec=pltpu.PrefetchScalarGridSpec(
            num_scalar_prefetch=1, grid=(nb, steps), in_specs=[own, half], out_specs=half),
        out_shape=jax.ShapeDtypeStruct(_half_shape(full.shape), BF16),
        compiler_params=_cparams(dimension_semantics=("parallel", "parallel")),
    )(c_arr, full, recv)


def _adam_math(g, w, m, v):
    m1 = ADAM_B1 * m + (1.0 - ADAM_B1) * g
    v1 = ADAM_B2 * v + (1.0 - ADAM_B2) * (g * g)
    m_hat = m1 / (1.0 - ADAM_B1 ** ADAM_STEP)
    v_hat = v1 / (1.0 - ADAM_B2 ** ADAM_STEP)
    delta = -ADAM_LR * (m_hat / (jnp.sqrt(v_hat) + ADAM_EPS) + ADAM_WD * w)
    return delta, m1, v1


def _adamw_halves(c_arr, own, other, w, m, v, name):
    rows, cols = w.shape
    by_rows = own.shape[1] == cols

    def body(c_ref, own_ref, oth_ref, w_ref, m_ref, v_ref, g_out, d_out, m_out, v_out):
        if by_rows:
            g = jnp.where(pl.program_id(0) == c_ref[0], own_ref[...], oth_ref[...])
        else:
            own_, oth_ = own_ref[...], oth_ref[...]
            g = jnp.where(c_ref[0] == 0, jnp.concatenate([own_, oth_], axis=1), jnp.concatenate([oth_, own_], axis=1))
        d, m1, v1 = _adam_math(g, w_ref[...], m_ref[...], v_ref[...])
        g_out[...] = g
        d_out[...] = d
        m_out[...] = m1
        v_out[...] = v1

    if by_rows:
        h = rows // 2
        tr = _row_tile(h)
        steps = h // tr
        grid = (2, steps)
        half_spec = pl.BlockSpec((tr, cols), lambda p, i, c_ref: (i, 0))
        full_spec = pl.BlockSpec((tr, cols), lambda p, i, c_ref: (p * steps + i, 0))
    else:
        tr = _row_tile(rows)
        grid = (1, rows // tr)
        half_spec = pl.BlockSpec((tr, cols // 2), lambda p, i, c_ref: (i, 0))
        full_spec = pl.BlockSpec((tr, cols), lambda p, i, c_ref: (i, 0))
    return pl.pallas_call(
        body, name=name,
        grid_spec=pltpu.PrefetchScalarGridSpec(
            num_scalar_prefetch=1, grid=grid,
            in_specs=[half_spec, half_spec, full_spec, full_spec, full_spec],
            out_specs=[full_spec] * 4),
        out_shape=[jax.ShapeDtypeStruct(w.shape, F32)] * 4,
        compiler_params=_cparams(dimension_semantics=("parallel", "parallel")),
    )(c_arr, own, other, w, m, v)


def _adamw_whole(items, name):
    n = len(items)

    def body(*refs):
        ins, outs = refs[:4 * n], refs[4 * n:]
        for k in range(n):
            g, w, m, v = (r[...] for r in ins[4 * k:4 * k + 4])
            d, m1, v1 = _adam_math(g, w, m, v)
            outs[3 * k][...] = d
            outs[3 * k + 1][...] = m1
            outs[3 * k + 2][...] = v1

    flat = [a for it in items for a in it]
    shapes = [jax.ShapeDtypeStruct(it[1].shape, F32) for it in items for _ in range(3)]
    out = pl.pallas_call(
        body, name=name, out_shape=shapes,
        in_specs=[VMEM_SPEC] * (4 * n), out_specs=[VMEM_SPEC] * (3 * n),
        compiler_params=_cparams(),
    )(*flat)
    return [tuple(out[3 * k:3 * k + 3]) for k in range(n)]


def _adamw_tiled(g, w, m, v, name):
    rows, cols = w.shape
    tr = _row_tile(rows)

    def body(g_ref, w_ref, m_ref, v_ref, d_out, m_out, v_out):
        d, m1, v1 = _adam_math(g_ref[...], w_ref[...], m_ref[...], v_ref[...])
        d_out[...] = d
        m_out[...] = m1
        v_out[...] = v1

    spec = pl.BlockSpec((tr, cols), lambda i: (i, 0))
    return pl.pallas_call(
        body, name=name, grid=(rows // tr,),
        out_shape=[jax.ShapeDtypeStruct(w.shape, F32)] * 3,
        in_specs=[spec] * 4, out_specs=[spec] * 3,
        compiler_params=_cparams(dimension_semantics=("parallel",)),
    )(g, w, m, v)


def _mod_forward(cond, w_mod, b_mod_cols, name):
    def body(c_ref, w_ref, b_ref, o_ref):
        o_ref[...] = _dot(_silu(c_ref[...]), w_ref[...]) + b_ref[...]

    return pl.pallas_call(
        body, name=name, out_shape=jax.ShapeDtypeStruct((cond.shape[0], w_mod.shape[1]), F32),
        in_specs=[VMEM_SPEC] * 3, out_specs=VMEM_SPEC, compiler_params=_cparams(),
    )(cond, w_mod, b_mod_cols)


def _mod_backward(cond, w_mod, dmod_cols, name):
    def body(c_ref, w_ref, d_ref, gw_ref, gc_ref):
        s = _silu(c_ref[...])
        d = d_ref[...]
        gw_ref[...] = _dot_tn(s, d)
        gc_ref[...] = _dot_nt(d[8:16, :], w_ref[...])

    return pl.pallas_call(
        body, name=name,
        out_shape=[jax.ShapeDtypeStruct(w_mod.shape, F32), jax.ShapeDtypeStruct((8, w_mod.shape[0]), F32)],
        in_specs=[VMEM_SPEC] * 3, out_specs=[VMEM_SPEC] * 2, compiler_params=_cparams(),
    )(cond, w_mod, dmod_cols)


def _col_chunks(width, step=512):
    return [(s, min(step, width - s)) for s in range(0, width, step)]


def _w_in_row(p_off):
    if p_off < 9 * HW:
        return p_off
    return 9 * HW if p_off == OFF_LR else p_off + 2 * RANK


def _in_projection(ctx0, x0, modc, modx, pre1, w_t, n_ctx_tiles, name):
    d = x0.shape[1]
    rows = ctx0.shape[0] + x0.shape[0]
    width = P_WIDTH

    def body(ctx_ref, x_ref, modc_ref, modx_ref, pre_ref, w_ref, h_ref, p_ref):
        is_ctx = pl.program_id(0) < n_ctx_tiles
        n, _ = _rms(jnp.where(is_ctx, ctx_ref[...], x_ref[...]))
        shift = jnp.where(is_ctx, modc_ref[0:1, :], modx_ref[0:1, :])
        scale = jnp.where(is_ctx, modc_ref[1:2, :], modx_ref[1:2, :])
        h = (n * pre_ref[...] * (1.0 + scale) + shift).astype(BF16)
        h_ref[...] = h
        for s, w in _col_chunks(width):
            p_ref[:, s:s + w] = _dot_nt(h, w_ref[_w_in_row(s):_w_in_row(s) + w, :])

    row = lambda i: (i, 0)
    fixed = lambda i: (0, 0)
    return pl.pallas_call(
        body, name=name, grid=(rows // TM,),
        out_shape=[jax.ShapeDtypeStruct((rows, d), BF16), jax.ShapeDtypeStruct((rows, width), F32)],
        in_specs=[pl.BlockSpec((TM, d), lambda i: (jnp.minimum(i, n_ctx_tiles - 1), 0)),
                  pl.BlockSpec((TM, d), lambda i: (jnp.maximum(i - n_ctx_tiles, 0), 0)),
                  pl.BlockSpec((8, d), fixed), pl.BlockSpec((8, d), fixed), pl.BlockSpec((1, d), fixed), VMEM_SPEC],
        out_specs=[pl.BlockSpec((TM, d), row), pl.BlockSpec((TM, width), row)],
        compiler_params=_cparams(dimension_semantics=("parallel",)),
    )(ctx0, x0, modc, modx, pre1, w_t)


C_HQ, C_HI, C_HF_FW, C_HF_BW, C_HGATE, C_GQ, C_GK, C_GV, C_GGATE = range(9)
OFF_GATE_HG = 9 * HW
OFF_LR = 13 * HW
P_WIDTH = OFF_LR + 128


def _head_norm_fwd(o, w):
    outs, ns, rs = [], [], []
    for h in range(NH):
        n, r = _rms(o[:, h * HD:(h + 1) * HD])
        ns.append(n)
        rs.append(r)
        outs.append(n * w)
    return jnp.concatenate(outs, axis=1), ns, rs


def _mixer_tail(z, o_hg, o_gla, p_hgate, p_ggate, p_gate_hg, p_gate_gla, hg_on, gla_on, wbh, wbg, wout):
    on_hg, n_hg, r_hg = _head_norm_fwd(o_hg, hg_on)
    on_gla, n_gla, r_gla = _head_norm_fwd(o_gla, gla_on)
    og_hg = (on_hg * _silu(p_hgate)).astype(BF16)
    og_gla = (on_gla * _silu(p_ggate)).astype(BF16)
    b_hg = jnp.dot(og_hg, wbh, preferred_element_type=F32)
    b_gla = jnp.dot(og_gla, wbg, preferred_element_type=F32)
    s_hg = _sigmoid(p_gate_hg)
    s_gla = _sigmoid(p_gate_gla)
    merged = (s_hg * b_hg + s_gla * b_gla).astype(BF16)
    y1 = jnp.dot(merged, wout, preferred_element_type=F32)
    return dict(on_hg=on_hg, n_hg=n_hg, r_hg=r_hg, on_gla=on_gla, n_gla=n_gla, r_gla=r_gla, og_hg=og_hg,
                og_gla=og_gla, b_hg=b_hg, b_gla=b_gla, s_hg=s_hg, s_gla=s_gla, merged=merged, y1=y1)


def _mixer_tail_fwd(x_lat, p, o_list, modx, norms, onorms, w_br_hg, w_br_gla, w_out, n_ctx_tiles, name):
    rows, d = x_lat.shape

    def body(x_ref, ofw_hg, obw_hg, ofw_gla, obw_gla, p_hgate, p_ggate, p_ghg_a, p_ghg_b, p_ggla_a, p_ggla_b,
             modx_ref, norm_ref, on_ref, wbh_ref, wbg_ref, wout_ref, z2_ref, y1_ref, mrg_ref, oghg_ref, oggla_ref):
        p_gate_hg = jnp.concatenate([p_ghg_a[...], p_ghg_b[...]], axis=1)
        p_gate_gla = jnp.concatenate([p_ggla_a[...], p_ggla_b[...]], axis=1)
        t = _mixer_tail(x_ref[...], ofw_hg[...] + obw_hg[...], ofw_gla[...] + obw_gla[...], p_hgate[...],
                        p_ggate[...], p_gate_hg, p_gate_gla, on_ref[0:1, 0:HD], on_ref[1:2, 0:HD],
                        wbh_ref[...], wbg_ref[...], wout_ref[...])
        y1_ref[...] = t["y1"]
        mrg_ref[...] = t["merged"]
        oghg_ref[...] = t["og_hg"]
        oggla_ref[...] = t["og_gla"]
        n1, _ = _rms(t["y1"])
        z2_ref[...] = x_ref[...] + n1 * norm_ref[1:2, :] * modx_ref[2:3, :]

    lat = lambda i: (i, 0)
    full = lambda i: (i + n_ctx_tiles, 0)
    fixed = lambda i: (0, 0)

    def pcol(blk):
        return pl.BlockSpec((TM, HW), lambda i: (i + n_ctx_tiles, blk))

    in_specs = ([pl.BlockSpec((TM, d), lat)] + [pl.BlockSpec((TM, HW), full)] * 4
                + [pcol(C_HGATE), pcol(C_GGATE), pcol(9), pcol(10), pcol(11), pcol(12)]
                + [pl.BlockSpec((8, d), fixed)] * 3 + [VMEM_SPEC] * 3)
    bf = lambda w: jax.ShapeDtypeStruct((rows, w), BF16)
    f32 = jax.ShapeDtypeStruct((rows, d), F32)
    return pl.pallas_call(
        body, name=name, grid=(rows // TM,), out_shape=[f32, f32, bf(d), bf(HW), bf(HW)], in_specs=in_specs,
        out_specs=[pl.BlockSpec((TM, d), lat)] * 3 + [pl.BlockSpec((TM, HW), lat)] * 2,
        compiler_params=_cparams(dimension_semantics=("parallel",)),
    )(x_lat, *o_list, p, p, p, p, p, p, modx, norms, onorms, w_br_hg, w_br_gla, w_out)


def _ffn_fwd_bwd(z2, modx, norms, w_gate, w_up, w_down, target, name):
    rows, d = z2.shape
    dff = w_gate.shape[0]
    inv_d = 1.0 / d

    def body(z2_ref, modx_ref, norm_ref, wg_ref, wu_ref, wd_ref, t_ref,
             loss_ref, dz2_ref, h2_ref, a_ref, du_ref, dv_ref, dy2_ref, stat_ref):
        i = pl.program_id(0)
        pre2, post2 = norm_ref[2:3, :], norm_ref[3:4, :]
        shift2, scale2, gate2 = modx_ref[3:4, :], modx_ref[4:5, :], modx_ref[5:6, :]
        z2 = z2_ref[...]
        n2, r2 = _rms(z2)
        nw2 = n2 * pre2
        h2 = (nw2 * (1.0 + scale2) + shift2).astype(BF16)
        h2_ref[...] = h2
        u = _dot_nt(h2, wg_ref[...])
        v = _dot_nt(h2, wu_ref[...])
        su = _silu(u)
        a = (su * v).astype(BF16)
        a_ref[...] = a
        y2 = jnp.dot(a, wd_ref[...], preferred_element_type=F32)
        n3, r3 = _rms(y2)
        err = z2 + n3 * post2 * gate2 - t_ref[...]
        part = 0.5 * inv_d * jnp.sum(err * err)
        dz3 = err * inv_d
        dgate2 = _colsum(dz3 * n3 * post2)
        tt = dz3 * gate2
        dpost2 = _colsum(tt * n3)
        dy2 = _rms_bwd(tt * post2, n3, r3).astype(BF16)
        dy2_ref[...] = dy2
        da = _dot_nt(dy2, wd_ref[...])
        du = (da * v * _dsilu(u)).astype(BF16)
        dv = (da * su).astype(BF16)
        du_ref[...] = du
        dv_ref[...] = dv
        dh2 = (jnp.dot(du, wg_ref[...], preferred_element_type=F32)
               + jnp.dot(dv, wu_ref[...], preferred_element_type=F32))
        dshift2 = _colsum(dh2)
        dscale2 = _colsum(dh2 * nw2)
        dnw2 = dh2 * (1.0 + scale2)
        dpre2 = _colsum(dnw2 * n2)
        dz2_ref[...] = dz3 + _rms_bwd(dnw2 * pre2, n2, r2)

        @pl.when(i == 0)
        def _():
            stat_ref[...] = jnp.zeros_like(stat_ref)
            loss_ref[...] = jnp.zeros_like(loss_ref)

        for r, val in enumerate((dshift2, dscale2, dgate2, dpre2, dpost2)):
            stat_ref[r:r + 1, :] += val
        loss_ref[...] += part
        stat_ref[5:6, 0:128] += part

    lat = lambda i: (i, 0)
    fixed = lambda i: (0, 0)
    bf = lambda w: jax.ShapeDtypeStruct((rows, w), BF16)
    return pl.pallas_call(
        body, name=name, grid=(rows // TM,),
        out_shape=[jax.ShapeDtypeStruct((8, 128), F32), jax.ShapeDtypeStruct((rows, d), F32), bf(d), bf(dff), bf(dff),
                   bf(dff), bf(d), jax.ShapeDtypeStruct((8, d), F32)],
        in_specs=[pl.BlockSpec((TM, d), lat), pl.BlockSpec((8, d), fixed), pl.BlockSpec((8, d), fixed)]
        + [VMEM_SPEC] * 3 + [pl.BlockSpec((TM, d), lat)],
        out_specs=[pl.BlockSpec((8, 128), fixed), pl.BlockSpec((TM, d), lat), pl.BlockSpec((TM, d), lat),
                   pl.BlockSpec((TM, dff), lat), pl.BlockSpec((TM, dff), lat), pl.BlockSpec((TM, dff), lat),
                   pl.BlockSpec((TM, d), lat), pl.BlockSpec((8, d), fixed)],
        compiler_params=_cparams(dimension_semantics=("arbitrary",)),
    )(z2, modx, norms, w_gate, w_up, w_down, target)


def _mixer_tail_bwd(x_lat, p, o_list, dz2, y1, modx, norms, onorms, w_br_hg, w_br_gla, w_out, n_ctx_tiles, n_tiles,
                    name):
    rows, d = x_lat.shape
    total = n_tiles * TM

    def body(x_ref, ofw_hg, obw_hg, ofw_gla, obw_gla, p_hgate, p_ggate, p_ghg_a, p_ghg_b, p_ggla_a, p_ggla_b,
             dz2_ref, y1_ref, modx_ref, norm_ref, on_ref, wbh_ref, wbg_ref, wout_ref,
             dohg_ref, dogla_ref, dhgate_ref, dggate_ref, dghg_ref, dggla_ref, dy1_ref, dbhg_ref, dbgla_ref,
             stat_ref):
        i = pl.program_id(0)

        @pl.when(i == 0)
        def _():
            stat_ref[...] = jnp.zeros_like(stat_ref)

        @pl.when(i < n_ctx_tiles)
        def _():
            for ref in (dohg_ref, dogla_ref, dhgate_ref, dggate_ref, dghg_ref, dggla_ref):
                ref[...] = jnp.zeros_like(ref)

        @pl.when(i >= n_ctx_tiles)
        def _():
            post1, gate1 = norm_ref[1:2, :], modx_ref[2:3, :]
            hg_on, gla_on = on_ref[0:1, 0:HD], on_ref[1:2, 0:HD]
            p_gate_hg = jnp.concatenate([p_ghg_a[...], p_ghg_b[...]], axis=1)
            p_gate_gla = jnp.concatenate([p_ggla_a[...], p_ggla_b[...]], axis=1)
            ph, pg = p_hgate[...], p_ggate[...]
            t = _mixer_tail(x_ref[...], ofw_hg[...] + obw_hg[...], ofw_gla[...] + obw_gla[...], ph, pg,
                            p_gate_hg, p_gate_gla, hg_on, gla_on, wbh_ref[...], wbg_ref[...], wout_ref[...])
            dz2 = dz2_ref[...]
            n1, r1 = _rms(y1_ref[...])
            dgate1 = _colsum(dz2 * n1 * post1)
            tt = dz2 * gate1
            dpost1 = _colsum(tt * n1)
            dy1 = _rms_bwd(tt * post1, n1, r1).astype(BF16)
            dy1_ref[...] = dy1
            dmerged = _dot_nt(dy1, wout_ref[...])
            dghg_ref[...] = (dmerged * t["b_hg"] * t["s_hg"] * (1.0 - t["s_hg"])).astype(BF16)
            dggla_ref[...] = (dmerged * t["b_gla"] * t["s_gla"] * (1.0 - t["s_gla"])).astype(BF16)
            db_hg = (dmerged * t["s_hg"]).astype(BF16)
            db_gla = (dmerged * t["s_gla"]).astype(BF16)
            dbhg_ref[...] = db_hg
            dbgla_ref[...] = db_gla
            don_acc = []
            for (db, wb, pgate, on, ns, rs, gain, gate_ref, do_ref) in (
                    (db_hg, wbh_ref, ph, t["on_hg"], t["n_hg"], t["r_hg"], hg_on, dhgate_ref, dohg_ref),
                    (db_gla, wbg_ref, pg, t["on_gla"], t["n_gla"], t["r_gla"], gla_on, dggate_ref, dogla_ref)):
                dog = _dot_nt(db, wb[...])
                gate_ref[...] = (dog * on * _dsilu(pgate)).astype(BF16)
                don = dog * _silu(pgate)
                acc = jnp.zeros((1, HD), F32)
                for h in range(NH):
                    sl = slice(h * HD, (h + 1) * HD)
                    acc = acc + _colsum(don[:, sl] * ns[h])
                    do_ref[:, sl] = _rms_bwd(don[:, sl] * gain, ns[h], rs[h]).astype(BF16)
                don_acc.append(acc)
            stat_ref[0:1, :] += dgate1
            stat_ref[1:2, :] += dpost1
            stat_ref[2:3, 0:HD] += don_acc[0]
            stat_ref[2:3, HD:2 * HD] += don_acc[1]

    lat = lambda i: (jnp.maximum(i - n_ctx_tiles, 0), 0)
    full = lambda i: (i, 0)
    fixed = lambda i: (0, 0)

    def pcol(blk):
        return pl.BlockSpec((TM, HW), lambda i: (i, blk))

    in_specs = ([pl.BlockSpec((TM, d), lat)] + [pl.BlockSpec((TM, HW), full)] * 4
                + [pcol(C_HGATE), pcol(C_GGATE), pcol(9), pcol(10), pcol(11), pcol(12)]
                + [pl.BlockSpec((TM, d), lat), pl.BlockSpec((TM, d), lat)]
                + [pl.BlockSpec((8, d), fixed)] * 3 + [VMEM_SPEC] * 3)
    f = lambda w: jax.ShapeDtypeStruct((total, w), BF16)
    out_shape = [f(HW), f(HW), f(HW), f(HW), f(d), f(d), jax.ShapeDtypeStruct((rows, d), BF16),
                 jax.ShapeDtypeStruct((rows, d), BF16), jax.ShapeDtypeStruct((rows, d), BF16),
                 jax.ShapeDtypeStruct((8, d), F32)]
    out_specs = ([pl.BlockSpec((TM, HW), full)] * 4 + [pl.BlockSpec((TM, d), full)] * 2
                 + [pl.BlockSpec((TM, d), lat)] * 3 + [pl.BlockSpec((8, d), fixed)])
    return pl.pallas_call(
        body, name=name, grid=(n_tiles,), out_shape=out_shape, in_specs=in_specs, out_specs=out_specs,
        compiler_params=_cparams(dimension_semantics=("arbitrary",)),
    )(x_lat, *o_list, p, p, p, p, p, p, dz2, y1, modx, norms, onorms, w_br_hg, w_br_gla, w_out)


def _in_projection_bwd(ctx0, x0, dz2, modc, modx, pre1, w_t, pieces, n_ctx_tiles, name):
    d = x0.shape[1]
    rows = ctx0.shape[0] + x0.shape[0]
    lat_rows = dz2.shape[0]
    width = P_WIDTH
    n_pieces = len(pieces)

    def body(*refs):
        ctx_ref, x_ref, dz2_ref, modc_ref, modx_ref, pre_ref, w_ref = refs[:7]
        (dhq_f, dhq_b, dhi_f, dhi_b, dhf_f, dhf_b, dhgate, dgq_f, dgq_b, dgk_f, dgk_b, dgv_f, dgv_b, dggate,
         dghg, dggla, dlr_f, dlr_b) = refs[7:7 + n_pieces]
        dp_ref, gx_ref, stat_ref = refs[7 + n_pieces:]
        i = pl.program_id(0)
        is_ctx = i < n_ctx_tiles
        z = jnp.where(is_ctx, ctx_ref[...], x_ref[...])
        sections = [
            (0, dhq_f[...] + dhq_b[...]), (HW, dhi_f[...] + dhi_b[...]), (2 * HW, dhf_f[...]), (3 * HW, dhf_b[...]),
            (4 * HW, dhgate[...]), (5 * HW, dgq_f[...] + dgq_b[...]), (6 * HW, dgk_f[...] + dgk_b[...]),
            (7 * HW, dgv_f[...] + dgv_b[...]), (8 * HW, dggate[...]),
            (9 * HW, dghg[:, 0:HW]), (10 * HW, dghg[:, HW:2 * HW]),
            (11 * HW, dggla[:, 0:HW]), (12 * HW, dggla[:, HW:2 * HW]), (OFF_LR, dlr_f[...] + dlr_b[...])]
        dh = jnp.zeros((TM, d), F32)
        for off, val in sections:
            w = val.shape[1]
            vb = val.astype(BF16)
            dp_ref[off:off + w, :] = vb.T
            dh = dh + jnp.dot(vb, w_ref[_w_in_row(off):_w_in_row(off) + w, :], preferred_element_type=F32)
        n, r = _rms(z)
        pre = pre_ref[...]
        scale = jnp.where(is_ctx, modc_ref[1:2, :], modx_ref[1:2, :])
        nw = n * pre
        dshift = _colsum(dh)
        dscale = _colsum(dh * nw)
        dnw = dh * (1.0 + scale)
        dpre = _colsum(dnw * n)
        gx_ref[...] = dz2_ref[...] + _rms_bwd(dnw * pre, n, r)
        zero = jnp.zeros((1, d), F32)

        @pl.when(i == 0)
        def _():
            stat_ref[...] = jnp.zeros_like(stat_ref)

        stat_ref[0:1, :] += jnp.where(is_ctx, zero, dshift)
        stat_ref[1:2, :] += jnp.where(is_ctx, zero, dscale)
        stat_ref[2:3, :] += jnp.where(is_ctx, dshift, zero)
        stat_ref[3:4, :] += jnp.where(is_ctx, dscale, zero)
        stat_ref[4:5, :] += dpre

    full = lambda i: (i, 0)
    lat = lambda i: (jnp.maximum(i - n_ctx_tiles, 0), 0)
    fixed = lambda i: (0, 0)
    piece_specs = [pl.BlockSpec((TM, a.shape[1]), full) for a in pieces]
    in_specs = [pl.BlockSpec((TM, d), lambda i: (jnp.minimum(i, n_ctx_tiles - 1), 0)), pl.BlockSpec((TM, d), lat),
                pl.BlockSpec((TM, d), lat), pl.BlockSpec((8, d), fixed),
                pl.BlockSpec((8, d), fixed), pl.BlockSpec((1, d), fixed), VMEM_SPEC] + piece_specs
    return pl.pallas_call(
        body, name=name, grid=(rows // TM,),
        out_shape=[jax.ShapeDtypeStruct((width, rows), BF16), jax.ShapeDtypeStruct((lat_rows, d), F32),
                   jax.ShapeDtypeStruct((8, d), F32)],
        in_specs=in_specs,
        out_specs=[pl.BlockSpec((width, TM), lambda i: (0, i)), pl.BlockSpec((TM, d), lat),
                   pl.BlockSpec((8, d), fixed)],
        compiler_params=_cparams(dimension_semantics=("arbitrary",)),
    )(ctx0, x0, dz2, modc, modx, pre1, w_t, *pieces)


def _transposed_lhs_matmul(x_ref, dy_ref, o_ref, xt_ref):
    @pl.when(pl.program_id(1) == 0)
    def _():
        xt_ref[...] = x_ref[...].T

    o_ref[...] = jnp.dot(xt_ref[...], dy_ref[...], preferred_element_type=F32)


def _w_in_grad(dp_t, h1, n_cols, name, after=()):
    rows, d = h1.shape
    n_main = OFF_LR // HW
    lr0 = _w_in_row(OFF_LR)

    def body(x_ref, xlr_ref, h_ref, *rest):
        o_hbm, acc_ref, sems = rest[len(after):]
        i = pl.program_id(0)
        slot = i % 2

        def main_copy(step):
            row = jnp.where(step < 9, step * HW, step * HW + 2 * RANK)
            return pltpu.make_async_copy(acc_ref.at[step % 2], o_hbm.at[pl.ds(pl.multiple_of(row, 8), HW), :],
                                         sems.at[step % 2])

        @pl.when(i > 1)
        def _():
            main_copy(i - 2).wait()

        @pl.when(i < n_main)
        def _():
            acc_ref[slot] = jnp.dot(x_ref[...], h_ref[...], preferred_element_type=F32)
            main_copy(i).start()

        @pl.when(i == n_main)
        def _():
            acc_ref[slot, 0:128, :] = jnp.dot(xlr_ref[...], h_ref[...], preferred_element_type=F32)
            lr_copy = pltpu.make_async_copy(acc_ref.at[slot, 0:2 * RANK, :], o_hbm.at[lr0:lr0 + 2 * RANK, :],
                                            sems.at[slot])
            lr_copy.start()
            main_copy(i - 1).wait()
            lr_copy.wait()

    return pl.pallas_call(
        body, name=name, grid=(n_main + 1,),
        out_shape=jax.ShapeDtypeStruct((n_cols, d), F32),
        in_specs=[pl.BlockSpec((HW, rows), lambda i: (jnp.minimum(i, n_main - 1), 0)),
                  pl.BlockSpec((128, rows), lambda i: (OFF_LR // 128, 0)),
                  pl.BlockSpec((rows, d), lambda i: (0, 0))] + [ANY_SPEC] * len(after),
        out_specs=ANY_SPEC,
        scratch_shapes=[pltpu.VMEM((2, HW, d), F32), pltpu.SemaphoreType.DMA((2,))],
        compiler_params=_cparams(dimension_semantics=("arbitrary",)),
    )(dp_t, dp_t, h1, *after)


def _weight_grad(xs, dy, name, tk=None, tn=512, k_first=0, k_tiles=None):
    rows = dy.shape[0]
    n = dy.shape[1]
    tn_ = min(tn, n)
    tk_ = xs.shape[1] if tk is None else tk
    k_tiles = xs.shape[1] // tk_ if k_tiles is None else k_tiles
    k = k_tiles * tk_

    return pl.pallas_call(
        functools.partial(_transposed_lhs_matmul), name=name, grid=(k_tiles, n // tn_),
        out_shape=jax.ShapeDtypeStruct((k, n), F32),
        in_specs=[pl.BlockSpec((rows, tk_), lambda i, j: (0, i + k_first)),
                  pl.BlockSpec((rows, tn_), lambda i, j: (0, j))],
        out_specs=pl.BlockSpec((tk_, tn_), lambda i, j: (i, j)),
        scratch_shapes=[pltpu.VMEM((tk_, rows), BF16)],
        compiler_params=_cparams(dimension_semantics=("parallel", "arbitrary")),
    )(xs, dy)


def _running_sums(xs, fws):
    c = xs[0].shape[0]
    row = lax.broadcasted_iota(jnp.int32, (c, 1), 0)
    s = 1
    while s < c:
        xs = [x + (jnp.where(row >= s, pltpu.roll(x, s, axis=0), 0.0) if fw else
                   jnp.where(row < c - s, pltpu.roll(x, c - s, axis=0), 0.0)) for x, fw in zip(xs, fws)]
        s *= 2
    return xs


def _chunks_terms(qs, ks, gs, fws):
    c = CHUNK
    n = len(qs)
    r = lax.broadcasted_iota(jnp.int32, (c, c), 0)
    s = lax.broadcasted_iota(jnp.int32, (c, c), 1)
    row = lax.broadcasted_iota(jnp.int32, (c, 1), 0)
    per_dir = {}
    for fw in set(fws):
        pos = row if fw else (c - 1 - row)
        per_dir[fw] = dict(
            causal=(s <= r) if fw else (s >= r), causal_t=(s >= r) if fw else (s <= r), pos=pos,
            in_blk=[(pos >= SUB * j) & (pos < SUB * (j + 1)) for j in range(NSUB)],
            start_row=[None] + [SUB * j - 1 if fw else c - SUB * j for j in range(1, NSUB)],
            rend=c - 1 if fw else 0)
    dirs = [per_dir[fw] for fw in fws]
    cums = _running_sums(gs, fws)
    starts = [[None] + [cum[d["start_row"][j]:d["start_row"][j] + 1, :] for j in range(1, NSUB)]
              for cum, d in zip(cums, dirs)]
    es = [[jnp.exp(cum) for cum in cums]]
    for j in range(1, NSUB):
        es.append([jnp.exp(jnp.where(d["pos"] >= SUB * j, cum - st[j], -1e30)) for cum, st, d in zip(cums, starts, dirs)])
    owns = [functools.reduce(lambda rest, j: jnp.where(d["in_blk"][j], st[j], rest), range(1, NSUB), 0.0)
            for st, d in zip(starts, dirs)]
    kscales = [jnp.exp(own - cum) for own, cum in zip(owns, cums)]
    cends = [cum[d["rend"]:d["rend"] + 1, :] for cum, d in zip(cums, dirs)]
    tails = [jnp.exp(cend - cum) for cend, cum in zip(cends, cums)]
    qcats = [jnp.concatenate([q * es[j][i] for j in range(NSUB)], axis=1).astype(BF16) for i, q in enumerate(qs)]
    kts = [k * ksc for k, ksc in zip(ks, kscales)]
    kms = [jnp.concatenate([jnp.where(d["in_blk"][j], kt, 0.0) for j in range(NSUB)], axis=1).astype(BF16)
           for kt, d in zip(kts, dirs)]
    e_by_lane = [[es[j][i] for j in range(NSUB)] for i in range(n)]
    return dict(dirs=dirs, e=e_by_lane, kscale=kscales, cend=cends, tail=tails, qcat=qcats, km=kms, kt=kts)


def _chunks_fwd(qs, ks, vs, gs, st0s, fws):
    t = _chunks_terms(qs, ks, gs, fws)
    scores = [_dot_nt(qc, km) for qc, km in zip(t["qcat"], t["km"])]
    a = [jnp.where(d["causal"], sc, 0.0) for sc, d in zip(scores, t["dirs"])]
    inter = [_dot_nt(qc[:, 0:HD], st0) for qc, st0 in zip(t["qcat"], st0s)]
    intra = [_dot(a_, v) for a_, v in zip(a, vs)]
    os_ = [x + y for x, y in zip(intra, inter)]
    upd = [_dot_tn(v, k * tl) for v, k, tl in zip(vs, ks, t["tail"])]
    st1s = [st0 * jnp.exp(ce) + u for st0, ce, u in zip(st0s, t["cend"], upd)]
    return os_, st1s


def _chunks_bwd(qs, ks, vs, gs, st0s, dos, dst1s, fws):
    n = len(qs)
    t = _chunks_terms(qs, ks, gs, fws)
    qcat, km, e, dirs = t["qcat"], t["km"], t["e"], t["dirs"]
    a_t = [jnp.where(d["causal_t"], _dot_nt(km_, qc), 0.0) for km_, qc, d in zip(km, qcat, dirs)]
    ktail = [k * tl for k, tl in zip(ks, t["tail"])]
    dv_a = [_dot(at, do) for at, do in zip(a_t, dos)]
    dv_b = [_dot_nt(kt, ds) for kt, ds in zip(ktail, dst1s)]
    dv = [x + y for x, y in zip(dv_a, dv_b)]
    da = [jnp.where(d["causal"], _dot_nt(do, v), 0.0) for do, v, d in zip(dos, vs, dirs)]
    da_t = [jnp.where(d["causal_t"], _dot_nt(v, do), 0.0) for do, v, d in zip(dos, vs, dirs)]
    dqcat = [_dot(da_, km_) for da_, km_ in zip(da, km)]
    dq_inter = [e[i][0] * _dot(dos[i], st0s[i]) for i in range(n)]
    dkm = [_dot(dat, qc) for dat, qc in zip(da_t, qcat)]
    dk_inter = [_dot(v, ds) * tl for v, ds, tl in zip(vs, dst1s, t["tail"])]
    dq = [dq_inter[i] + sum(e[i][j] * dqcat[i][:, j * HD:(j + 1) * HD] for j in range(NSUB)) for i in range(n)]
    dkt = [sum(jnp.where(dirs[i]["in_blk"][j], dkm[i][:, j * HD:(j + 1) * HD], 0.0) for j in range(NSUB))
           for i in range(n)]
    dk = [dkt[i] * t["kscale"][i] + dk_inter[i] for i in range(n)]
    dcum = [qs[i] * dq_inter[i] - ks[i] * dk_inter[i] - t["kt"][i].astype(BF16).astype(F32) * dkt[i]
            + sum(qcat[i][:, j * HD:(j + 1) * HD].astype(F32) * dqcat[i][:, j * HD:(j + 1) * HD] for j in range(NSUB))
            for i in range(n)]
    ecend = [jnp.exp(ce) for ce in t["cend"]]
    end = [ecend[i] * _colsum(st0s[i] * dst1s[i]) + _colsum(ks[i] * dk_inter[i]) for i in range(n)]
    sums = _running_sums(dcum, [not fw for fw in fws])
    dg = [sm + en for sm, en in zip(sums, end)]
    upd = [_dot_tn(dos[i], qs[i] * e[i][0]) for i in range(n)]
    dst0 = [dst1s[i] * ecend[i] + upd[i] for i in range(n)]
    return dq, dk, dv, dg, dst0


def _chunk_index(step, n_ctx_chunks, n_chunks, fw):
    if fw:
        return step
    return jnp.where(step < n_ctx_chunks, n_ctx_chunks - 1 - step, n_chunks - 1 + n_ctx_chunks - step)


def _hg_inputs(hq, hf, lbv, d_idx, sl):
    lb = _sigmoid(lbv[d_idx:d_idx + 1, sl] - lbv[2 + d_idx:3 + d_idx, sl])
    sg = _sigmoid(hf)
    f = lb + (1.0 - lb) * sg
    return _silu(hq), 1.0 - f, jnp.log(f), f, sg, lb


def _scan_fwd_both(p, branch_sides, n_ctx_chunks, name):
    rows = p.shape[0]
    n_chunks = rows // CHUNK
    n_ins = [4 if branch == "hg" else 6 for branch, _ in branch_sides]
    n_in_all = 2 * sum(n_ins)
    n_br = len(branch_sides)

    def body(*refs):
        ins, outs, state = refs[:n_in_all], refs[n_in_all:n_in_all + 4 * n_br], refs[-1]

        @pl.when(pl.program_id(0) == 0)
        def _():
            state[...] = jnp.zeros_like(state)

        lanes, where = [], []
        pos = 0
        for bi, (branch, _) in enumerate(branch_sides):
            hg = branch == "hg"
            n_in = n_ins[bi]
            for di, fw in enumerate((True, False)):
                r = ins[pos:pos + n_in]
                pos += n_in
                o_ref, st_ref = outs[4 * bi + 2 * di], outs[4 * bi + 2 * di + 1]
                if hg:
                    a_ref, b_ref, c_ref, lb_ref = r
                else:
                    a_ref, b_ref, c_ref, lr_ref, wgk_ref, bgk_ref = r
                    logits = _dot(lr_ref[...], wgk_ref[...]) + bgk_ref[...]
                    g_all = _log_sigmoid(logits) * (1.0 / GATE_NORM)
                for h in range(NH):
                    sl = slice(h * HD, (h + 1) * HD)
                    if hg:
                        q, k, g, _, _, _ = _hg_inputs(a_ref[:, sl], c_ref[:, sl], lb_ref[...], di, sl)
                        v = b_ref[:, sl]
                    else:
                        q, k, v, g = a_ref[:, sl] * (HD ** -0.5), b_ref[:, sl], c_ref[:, sl], g_all[:, sl]
                    lanes.append((q, k, v, g, state[2 * bi + di, h], fw))
                    where.append((2 * bi + di, h, sl, o_ref, st_ref))
        qs, ks, vs, gs, st0s, fws = (list(col) for col in zip(*lanes))
        os_, st1s = _chunks_fwd(qs, ks, vs, gs, st0s, fws)
        for (si, h, sl, o_ref, st_ref), st0, o, st1 in zip(where, st0s, os_, st1s):
            st_ref[0, h] = st0
            o_ref[:, sl] = o
            state[si, h] = st1

    fixed = lambda j: (0, 0)
    in_specs, args, out_specs = [], [], []
    for branch, side in branch_sides:
        for di, fw in enumerate((True, False)):
            chunk = functools.partial(_chunk_index, n_ctx_chunks=n_ctx_chunks, n_chunks=n_chunks, fw=fw)

            def cmap(blk, width=HW, chunk=chunk):
                return pl.BlockSpec((CHUNK, width), lambda j: (chunk(j), blk))

            if branch == "hg":
                in_specs += [cmap(C_HQ), cmap(C_HI), cmap(C_HF_FW + di), pl.BlockSpec((4, HW), fixed)]
                args += [p, p, p, side]
            else:
                in_specs += [cmap(C_GQ), cmap(C_GK), cmap(C_GV), cmap(OFF_LR // 128, 128),
                             pl.BlockSpec((128, HW), fixed), pl.BlockSpec((1, HW), fixed)]
                args += [p, p, p, p, side[di][0], side[di][1]]
            out_specs += [cmap(0), pl.BlockSpec((1, NH, HD, HD), lambda j, chunk=chunk: (chunk(j), 0, 0, 0))]
    return pl.pallas_call(
        body, name=name, grid=(n_chunks,),
        out_shape=[jax.ShapeDtypeStruct((rows, HW), F32),
                   jax.ShapeDtypeStruct((n_chunks, NH, HD, HD), F32)] * (2 * n_br),
        in_specs=in_specs, out_specs=out_specs,
        scratch_shapes=[pltpu.VMEM((2 * n_br, NH, HD, HD), F32)],
        compiler_params=_cparams(dimension_semantics=("arbitrary",)),
    )(*args)


def _scan_bwd_both(p, branch_items, n_ctx_chunks, name):
    rows = p.shape[0]
    n_chunks = rows // CHUNK
    n_ins = [6 if item[0] == "hg" else 8 for item in branch_items]
    n_outs = [4 if item[0] == "hg" else 6 for item in branch_items]
    n_in_all, n_out_all = 2 * sum(n_ins), 2 * sum(n_outs)

    def body(*refs):
        ins, outs, dstate = refs[:n_in_all], refs[n_in_all:n_in_all + n_out_all], refs[-1]
        first = pl.program_id(0) == 0

        @pl.when(first)
        def _():
            dstate[...] = jnp.zeros_like(dstate)

        lanes, where, extra, ctx = [], [], [], []
        ipos = opos = 0
        for bi, item in enumerate(branch_items):
            hg = item[0] == "hg"
            for di, fw in enumerate((True, False)):
                r, w = ins[ipos:ipos + n_ins[bi]], outs[opos:opos + n_outs[bi]]
                ipos += n_ins[bi]
                opos += n_outs[bi]
                if hg:
                    a_ref, b_ref, c_ref, lb_ref, st_ref, do_ref = r
                    acc_refs = (w[3],)
                else:
                    a_ref, b_ref, c_ref, lr_ref, wgk_ref, bgk_ref, st_ref, do_ref = r
                    acc_refs = (w[4], w[5])
                    lr = lr_ref[...]
                    logits = _dot(lr, wgk_ref[...]) + bgk_ref[...]
                    g_all = _log_sigmoid(logits) * (1.0 / GATE_NORM)

                @pl.when(first)
                def _(acc_refs=acc_refs):
                    for ref in acc_refs:
                        ref[...] = jnp.zeros_like(ref)

                for h in range(NH):
                    sl = slice(h * HD, (h + 1) * HD)
                    if hg:
                        hq, hf = a_ref[:, sl], c_ref[:, sl]
                        q, k, g, f, sg, lb = _hg_inputs(hq, hf, lb_ref[...], di, sl)
                        v = b_ref[:, sl]
                        extra.append((hq, f, sg, lb))
                    else:
                        q, k, v, g = a_ref[:, sl] * (HD ** -0.5), b_ref[:, sl], c_ref[:, sl], g_all[:, sl]
                        extra.append(None)
                    lanes.append((q, k, v, g, st_ref[0, h], do_ref[:, sl], dstate[2 * bi + di, h], fw))
                    where.append((2 * bi + di, h, sl))
                ctx.append((hg, w, None if hg else (lr, logits, wgk_ref)))

        dqs, dks, dvs, dgs, dst0s = [], [], [], [], []
        for lo in range(0, len(lanes), BWD_LANES):
            cols = [list(col) for col in zip(*lanes[lo:lo + BWD_LANES])]
            for acc, part in zip((dqs, dks, dvs, dgs, dst0s), _chunks_bwd(*cols)):
                acc.extend(part)
        dg_parts = [[] for _ in ctx]
        for (si, h, sl), ex, dq, dk, dv, dg, dst0 in zip(where, extra, dqs, dks, dvs, dgs, dst0s):
            dstate[si, h] = dst0
            hg, w, _ = ctx[si]
            if hg:
                hq, f, sg, lb = ex
                da_ref, db_ref, dc_ref, dlb_ref = w
                da_ref[:, sl] = (dq * _dsilu(hq)).astype(BF16)
                db_ref[:, sl] = dv.astype(BF16)
                df = dg / f - dk
                dc_ref[:, sl] = (df * (1.0 - lb) * sg * (1.0 - sg)).astype(BF16)
                dlb_ref[0:1, sl] += _colsum(df * (1.0 - sg))
            else:
                da_ref, db_ref, dc_ref = w[:3]
                da_ref[:, sl] = (dq * (HD ** -0.5)).astype(BF16)
                db_ref[:, sl] = dk.astype(BF16)
                dc_ref[:, sl] = dv.astype(BF16)
                dg_parts[si].append(dg)
        for si, (hg, w, more) in enumerate(ctx):
            if not hg:
                dlr_ref, dwgk_ref, dbias_ref = w[3:]
                lr, logits, wgk_ref = more
                dlogits = jnp.concatenate(dg_parts[si], axis=1) * (1.0 / GATE_NORM) * (1.0 - _sigmoid(logits))
                dlr_ref[...] = _dot_nt(dlogits, wgk_ref[...]).astype(BF16)
                dwgk_ref[...] += _dot_tn(lr, dlogits)
                dbias_ref[0:1, :] += _colsum(dlogits)

    fixed = lambda j: (0, 0)
    big = jax.ShapeDtypeStruct((rows, HW), BF16)
    in_specs, args, out_shape, out_specs = [], [], [], []
    for branch, side, states, d_o in branch_items:
        for di, fw in enumerate((True, False)):
            def chunk_of(j, fw=fw):
                return _chunk_index(n_chunks - 1 - j, n_ctx_chunks, n_chunks, fw)

            def cmap(blk, width=HW, chunk_of=chunk_of):
                return pl.BlockSpec((CHUNK, width), lambda j: (chunk_of(j), blk))

            st_spec = pl.BlockSpec((1, NH, HD, HD), lambda j, chunk_of=chunk_of: (chunk_of(j), 0, 0, 0))
            if branch == "hg":
                in_specs += [cmap(C_HQ), cmap(C_HI), cmap(C_HF_FW + di), pl.BlockSpec((4, HW), fixed), st_spec,
                             cmap(0)]
                args += [p, p, p, side, states[di], d_o]
                out_shape += [big, big, big, jax.ShapeDtypeStruct((8, HW), F32)]
                out_specs += [cmap(0), cmap(0), cmap(0), pl.BlockSpec((8, HW), fixed)]
            else:
                in_specs += [cmap(C_GQ), cmap(C_GK), cmap(C_GV), cmap(OFF_LR // 128, 128),
                             pl.BlockSpec((128, HW), fixed), pl.BlockSpec((1, HW), fixed), st_spec, cmap(0)]
                args += [p, p, p, p, side[di][0], side[di][1], states[di], d_o]
                out_shape += [big, big, big, jax.ShapeDtypeStruct((rows, 128), BF16),
                              jax.ShapeDtypeStruct((128, HW), F32), jax.ShapeDtypeStruct((8, HW), F32)]
                out_specs += [cmap(0), cmap(0), cmap(0), cmap(0, 128), pl.BlockSpec((128, HW), fixed),
                              pl.BlockSpec((8, HW), fixed)]
    return pl.pallas_call(
        body, name=name, grid=(n_chunks,), out_shape=out_shape, in_specs=in_specs, out_specs=out_specs,
        scratch_shapes=[pltpu.VMEM((2 * len(branch_items), NH, HD, HD), F32)],
        compiler_params=_cparams(dimension_semantics=("arbitrary",)),
    )(*args)


SMALL_ROWS = 56
ROWS_MOD_X = (0, 1, 8, 16, 17, 18)
ROWS_MOD_C = (2, 3)
ROW_PRE1, ROW_POST1, ROW_ONORM, ROW_PRE2, ROW_POST2, ROW_LB, ROW_BGK, ROW_WGK = 4, 9, 10, 19, 20, 24, 32, 40
ROW_LOSS = 21


def _reduce_small(gathered, lb_full, name):
    _, _, d = gathered.shape

    def body(g_ref, lb_ref, sum_ref, dmod_ref, dbmod_ref, dlb_ref):
        total = g_ref[0]
        for b in range(1, N_DEV):
            total = total + g_ref[b]
        sum_ref[...] = total
        dmod_ref[...] = jnp.zeros_like(dmod_ref)
        for m in range(N_MOD):
            col = slice(m * d, (m + 1) * d)
            acc = jnp.zeros((1, d), F32)
            for b in range(N_DEV):
                row = g_ref[b, ROWS_MOD_X[m]:ROWS_MOD_X[m] + 1, :]
                dmod_ref[b:b + 1, col] = row
                acc = acc + row
            if m < 2:
                ctx_row = total[ROWS_MOD_C[m]:ROWS_MOD_C[m] + 1, :]
                dmod_ref[8:9, col] = ctx_row
                acc = acc + ctx_row
            dbmod_ref[:, col] = acc
        lbv = lb_ref[...]
        for dd in range(2):
            lb = _sigmoid(lbv[dd:dd + 1, :] - lbv[2 + dd:3 + dd, :])
            gl = total[ROW_LB:ROW_LB + 1, dd * HW:(dd + 1) * HW] * lb * (1.0 - lb)
            dlb_ref[dd:dd + 1, :] = gl
            dlb_ref[2 + dd:3 + dd, :] = -gl

    return pl.pallas_call(
        body, name=name,
        out_shape=[jax.ShapeDtypeStruct((SMALL_ROWS, d), F32), jax.ShapeDtypeStruct((16, N_MOD * d), F32),
                   jax.ShapeDtypeStruct((1, N_MOD * d), F32), jax.ShapeDtypeStruct((4, HW), F32)],
        in_specs=[VMEM_SPEC] * 2, out_specs=[VMEM_SPEC] * 4, compiler_params=_cparams(),
    )(gathered, lb_full)


def _c_ctx_grad(gathered, c_ctx_row, name):
    def body(g_ref, c_ref, o_ref):
        acc = g_ref[0, 0:1, :]
        for chip in range(1, N_CHIP):
            acc = acc + g_ref[2 * chip, 0:1, :]
        o_ref[...] = acc * _dsilu(c_ref[...])

    return pl.pallas_call(
        body, name=name, out_shape=jax.ShapeDtypeStruct(c_ctx_row.shape, F32),
        in_specs=[VMEM_SPEC] * 2, out_specs=VMEM_SPEC, compiler_params=_cparams(),
    )(gathered, c_ctx_row)


def _blocked(full, n_blocks):
    k, n = full.shape
    return full.reshape(k, n_blocks, n // n_blocks).transpose(1, 0, 2)


def _unblocked(blocks):
    nb, k, n = blocks.shape
    return blocks.transpose(1, 0, 2).reshape(k, nb * n)


def _sample_front(x0, ctx0, modc, modx, norm_pre1, lb_full, gla_side, w_in_r):
    ctx_len = ctx0.shape[0]
    n_ctx_tiles = ctx_len // TM
    n_ctx_chunks = ctx_len // CHUNK
    h1, p = _in_projection(ctx0, x0, modc, modx, norm_pre1, w_in_r, n_ctx_tiles, "in_projection")
    (o_hg_fw, st_hg_fw, o_hg_bw, st_hg_bw, o_gla_fw, st_gla_fw, o_gla_bw, st_gla_bw) = _scan_fwd_both(
        p, [("hg", lb_full), ("gla", gla_side)], n_ctx_chunks, "scan_fwd")
    return dict(h1=h1, p=p, o_list=[o_hg_fw, o_hg_bw, o_gla_fw, o_gla_bw],
                states=[st_hg_fw, st_hg_bw, st_gla_fw, st_gla_bw])


def _sample_back(reduce, front, x0, ctx0, target0, modc, modx, norm_pre1, norms, onorms, lb_full, gla_side, w_in_r,
                 wbh, wbg, wout, ffn_weights):
    seq, d = x0.shape
    ctx_len = ctx0.shape[0]
    n_ctx_tiles = ctx_len // TM
    n_tiles = (ctx_len + seq) // TM
    n_ctx_chunks = ctx_len // CHUNK
    h1, p, o_list = front["h1"], front["p"], front["o_list"]
    st_hg_fw, st_hg_bw, st_gla_fw, st_gla_bw = front["states"]
    z2, y1, merged, og_hg, og_gla = _mixer_tail_fwd(x0, p, o_list, modx, norms, onorms, wbh, wbg, wout, n_ctx_tiles,
                                                    "mixer_tail")
    wg, wu, wd = ffn_weights([z2])
    loss_part, dz2, h2, a_act, du, dv, dy2, stat_ffn = _ffn_fwd_bwd(z2, modx, norms, wg, wu, wd, target0, "ffn")
    dff = wg.shape[0]
    tok = reduce("ffn", [_weight_grad(du, h2, "grad_w_ff_gate", tk=dff // 2, tn=d),
                         _weight_grad(dv, h2, "grad_w_ff_up", tk=dff // 2, tn=d),
                         _weight_grad(a_act, dy2, "grad_w_ff_down", tk=dff // 2)])

    (d_ohg, d_ogla, d_hgate, d_ggate, d_ghg, d_ggla, dy1, db_hg, db_gla, stat_mix) = _mixer_tail_bwd(
        x0, p, o_list, dz2, y1, modx + tok, norms, onorms, wbh, wbg, wout, n_ctx_tiles, n_tiles, "mixer_tail_bwd")
    tok = reduce("mix", [_weight_grad(og_hg, db_hg, "grad_w_br_hg"), _weight_grad(og_gla, db_gla, "grad_w_br_gla"),
                         _weight_grad(merged, dy1, "grad_w_out")])
    tok = tok + reduce("push_ffn", [dy1])
    gla_b = [(wgk, bias + tok) for wgk, bias in gla_side]
    (dgq_f, dgk_f, dgv_f, dlr_f, dwgk_f, dbgk_f, dgq_b, dgk_b, dgv_b, dlr_b, dwgk_b, dbgk_b,
     dhq_f, dhi_f, dhf_f, dlb_f, dhq_b, dhi_b, dhf_b, dlb_b) = _scan_bwd_both(
        p, [("gla", gla_b, (st_gla_fw, st_gla_bw), d_ogla), ("hg", lb_full, (st_hg_fw, st_hg_bw), d_ohg)],
        n_ctx_chunks, "scan_bwd")
    tok = reduce("push_mix", [dbgk_f])
    pieces = [dhq_f, dhq_b, dhi_f, dhi_b, dhf_f, dhf_b, d_hgate, dgq_f, dgq_b, dgk_f, dgk_b, dgv_f, dgv_b, d_ggate,
              d_ghg, d_ggla, dlr_f, dlr_b]
    dp, grad_x, stat_in = _in_projection_bwd(ctx0, x0, dz2, modc, modx, norm_pre1 + tok, w_in_r, pieces, n_ctx_tiles,
                                             "in_projection_bwd")

    started = reduce("small_start", dict(stat_in=stat_in, stat_mix=stat_mix, stat_ffn=stat_ffn, dlb=(dlb_f, dlb_b),
                                         dwgk=(dwgk_f, dwgk_b), dbgk=(dbgk_f, dbgk_b)))
    reduce("in", [_w_in_grad(dp, h1, w_in_r.shape[0], "grad_w_in", after=[started])])
    reduce("small", None)
    reduce("push_in", [])
    return dict(loss_part=loss_part, grad_x=grad_x)


def kernel(x, c, ctx, c_ctx, w_mod, b_mod, norm_pre1, norm_post1, norm_pre2, norm_post2, w_in, hg_lb, hg_onorm, gla_w_gk, gla_b_gk, gla_onorm, w_br_hg, w_br_gla, w_out, w_ff_gate, w_ff_up, w_ff_down, loss_target, m_c_ctx, m_w_mod, m_b_mod, m_norm_pre1, m_norm_post1, m_norm_pre2, m_norm_post2, m_w_in, m_hg_lb, m_hg_onorm, m_gla_w_gk, m_gla_b_gk, m_gla_onorm, m_w_br_hg, m_w_br_gla, m_w_out, m_w_ff_gate, m_w_ff_up, m_w_ff_down, v_c_ctx, v_w_mod, v_b_mod, v_norm_pre1, v_norm_post1, v_norm_pre2, v_norm_post2, v_w_in, v_hg_lb, v_hg_onorm, v_gla_w_gk, v_gla_b_gk, v_gla_onorm, v_w_br_hg, v_w_br_gla, v_w_out, v_w_ff_gate, v_w_ff_up, v_w_ff_down):
    seq, d = x.shape[1], x.shape[2]
    ctx_len = ctx.shape[1]
    assert seq % TM == 0 and ctx_len % TM == 0 and d == 2 * HW
    ax, ay, ac = lax.axis_index("x"), lax.axis_index("y"), lax.axis_index("c")
    chip = 2 * ax + ay
    dev = 2 * chip + ac
    c_arr = jnp.reshape(ac, (1,)).astype(jnp.int32)
    chip_arr = jnp.reshape(chip, (1,)).astype(jnp.int32)
    transposed = ("w_in", "w_ff_gate", "w_ff_up")
    view = lambda a, nm: a[0].T if nm in transposed else a[0]

    sems_in, lands_in, token_in0 = _blocks_start([_cast_into_blocks(chip_arr, view(w_in, "w_in"), "cast_w_in")],
                                                 "gather_w_in_start")

    nc = d // 128
    pad8 = lambda a: jnp.pad(a, ((0, -a.shape[0] % 8), (0, 0)))
    small1 = jnp.concatenate([c.reshape(nc, 128) + token_in0[0, 0], pad8(hg_lb.reshape(4, 128)),
                              gla_w_gk.reshape(2 * RANK, 128), pad8(gla_b_gk.reshape(2, 128))], axis=0)
    blocks = [_cast_into_blocks(chip_arr, view(w_, nm), "cast_" + nm) for w_, nm in (
        (w_br_hg, "w_br_hg"), (w_br_gla, "w_br_gla"), (w_out, "w_out"), (w_ff_gate, "w_ff_gate"),
        (w_ff_up, "w_ff_up"), (w_ff_down, "w_ff_down"))]
    got1 = _allgather8(small1, "gather_small_params", after=blocks)
    c_all = got1[:, :nc, :].reshape(N_DEV, d)
    per_chip = got1[0::2]
    lb_full = per_chip[:, nc:nc + 4, :].transpose(1, 0, 2).reshape(4, HW)
    wgk_full = per_chip[:, nc + 8:nc + 8 + 2 * RANK, :].transpose(1, 0, 2).reshape(2, RANK, HW)
    bgk_full = per_chip[:, nc + 8 + 2 * RANK:nc + 10 + 2 * RANK, :].transpose(1, 0, 2).reshape(2, HW)
    wgk_pad = [jnp.zeros((128, HW), F32).at[dd * RANK:(dd + 1) * RANK].set(wgk_full[dd]) for dd in range(2)]
    bgk = [bgk_full[dd:dd + 1] for dd in range(2)]

    n_mod_cols = w_mod.shape[2]
    cond = jnp.concatenate([c_all, pad8(c_ctx.reshape(1, d))], axis=0)
    b_cols = lax.dynamic_slice(b_mod, (0, chip * n_mod_cols), (1, n_mod_cols))
    lands_in = _blocks_wait(sems_in, lands_in, [got1], "gather_w_in_wait")
    fwd_sems, lands_in, fwd_token = _forward_start(lands_in, "gather_w_in_forward_start")
    mod_part = _mod_forward(cond + fwd_token[0, 0], w_mod[0], b_cols, "mod_forward")
    mod_got = _allgather8(mod_part, "gather_mod")
    mod_all = mod_got[0::2].transpose(1, 0, 2).reshape(16, N_CHIP * n_mod_cols)
    modx = pad8(lax.dynamic_slice(mod_all, (dev, 0), (1, N_MOD * d)).reshape(N_MOD, d))
    modc = pad8(mod_all[8].reshape(N_MOD, d))

    gathered_in = _forward_wait(fwd_sems, lands_in, [mod_got], "gather_w_in_forward_wait")
    sems, lands, token = _blocks_start(blocks, "gather_rest_start", after=[gathered_in[0]])
    w_in_r = gathered_in[0].reshape(-1, d)

    norms = jnp.concatenate([norm_pre1, norm_post1, norm_pre2, norm_post2, jnp.zeros((4, d), F32)], axis=0)
    onorms = jnp.zeros((8, d), F32).at[0, :HD].set(hg_onorm[0]).at[1, :HD].set(gla_onorm[0])
    gla_side = [(wgk_pad[dd], bgk[dd]) for dd in range(2)]
    modx = modx + token[0, 0]
    front = _sample_front(x[0], ctx[0], modc, modx, norm_pre1, lb_full, gla_side, w_in_r)
    lands = _blocks_wait(sems, lands, front["o_list"], "gather_rest_wait")
    gathered = _blocks_finish(lands[:3], "gather_mix_finish")
    wbh, wbg = _unblocked(gathered[0]), _unblocked(gathered[1])
    wout = gathered[2].reshape(d, d)
    ffn_sems, ffn_lands, ffn_token = _forward_start(lands[3:], "gather_ffn_forward_start")
    onorms = onorms + ffn_token[0, 0]

    def ffn_weights(after):
        got = _forward_wait(ffn_sems, ffn_lands, after, "gather_ffn_forward_wait")
        return tuple(g.reshape(-1, d) for g in got)

    dff = w_ff_down.shape[1] * N_CHIP
    groups = {"ffn": ["w_ff_gate", "w_ff_up", "w_ff_down"], "mix": ["w_br_hg", "w_br_gla", "w_out"], "in": ["w_in"]}
    row_sharded = {"w_out": d // N_CHIP, "w_ff_down": dff // N_CHIP, "w_ff_gate": dff // N_CHIP,
                   "w_ff_up": dff // N_CHIP, "w_in": w_in.shape[2]}
    in_flight, to_sibling, small = {}, {}, {}

    def reduce_small_start(stats):
        small2 = jnp.concatenate([
            stats["stat_in"], stats["stat_mix"], stats["stat_ffn"],
            jnp.concatenate(stats["dlb"], axis=1), jnp.concatenate(stats["dbgk"], axis=1),
            jnp.concatenate([stats["dwgk"][0][0:RANK], stats["dwgk"][1][RANK:2 * RANK]], axis=1)], axis=0)
        assert small2.shape[0] == SMALL_ROWS
        sems_, land_, token_ = _allgather8_start(small2, "gather_small_grads_start")
        small.update(gathering=(sems_, land_))
        return token_

    def reduce_small():
        got2 = _allgather8_wait(*small["gathering"], to_sibling["in"][1], "gather_small_grads_wait")
        total, dmod_all, g_b_mod, g_lb_full = _reduce_small(got2, lb_full, "reduce_small")
        dmod_cols = lax.dynamic_slice(dmod_all, (0, chip * n_mod_cols), (16, n_mod_cols))
        g_w_mod, cctx_part = _mod_backward(cond, w_mod[0], dmod_cols, "mod_backward")
        got3 = _allgather8(cctx_part, "gather_c_ctx_grad")
        g_c_ctx = _c_ctx_grad(got3, c_ctx.reshape(1, d), "c_ctx_grad")
        small.update(total=total, g_b_mod=g_b_mod, g_lb_full=g_lb_full, g_w_mod=g_w_mod, g_c_ctx=g_c_ctx)

    def reduce(group, grads):
        if group == "small_start":
            return reduce_small_start(grads)
        if group == "small":
            return reduce_small()
        if group.startswith("push_"):
            return push(group[5:], grads)
        nms = groups[group]
        full = [g.reshape(N_CHIP, row_sharded[nm], d) if nm in row_sharded else _blocked(g, N_CHIP)
                for g, nm in zip(grads, nms)]
        sems_, full, lands_, token_ = _send_half_start(full, "grads_to_sibling_start_" + group)
        to_sibling[group] = (sems_, full, lands_)
        return token_[0, 0]

    def push(group, after):
        nms = groups[group]
        sems_, full, lands_ = to_sibling[group]
        if group == "in":
            after = list(after) + [small["g_c_ctx"], small["total"]]
        full, from_sibling = _send_half_wait(sems_, full, lands_, after, "grads_to_sibling_wait_" + group)
        pairs = [_pair_sum(c_arr, f, r_, "pair_sum_" + nm) for f, r_, nm in zip(full, from_sibling, nms)]
        after = [small["g_c_ctx"], small["total"]] if group == "in" else []
        sems_, pairs, lands_, token_ = _scatter_start(pairs, "grads_to_owner_start_" + group, after)
        in_flight[group] = (sems_, pairs, lands_, token_)
        return token_[0, 0]

    r = _sample_back(reduce, front, x[0], ctx[0], loss_target[0], modc, modx, norm_pre1, norms, onorms, lb_full,
                     gla_side, w_in_r, wbh, wbg, wout, ffn_weights)
    grad_x = r["grad_x"]

    weights = dict(w_in=(w_in, m_w_in, v_w_in), w_br_hg=(w_br_hg, m_w_br_hg, v_w_br_hg),
                   w_br_gla=(w_br_gla, m_w_br_gla, v_w_br_gla), w_out=(w_out, m_w_out, v_w_out),
                   w_ff_gate=(w_ff_gate, m_w_ff_gate, v_w_ff_gate), w_ff_up=(w_ff_up, m_w_ff_up, v_w_ff_up),
                   w_ff_down=(w_ff_down, m_w_ff_down, v_w_ff_down))
    names = ["w_in", "w_br_hg", "w_br_gla", "w_out", "w_ff_gate", "w_ff_up", "w_ff_down"]
    big, swapping = {}, {}

    def sum_and_swap(group, after):
        sems_, pairs, lands_, _ = in_flight[group]
        pairs, lands_ = _scatter_wait(sems_, pairs, lands_, after, "grads_to_owner_wait_" + group)
        own_half = [_sum_owner(chip_arr, pr, g, "chip_sum_" + nm) for pr, g, nm in zip(pairs, lands_, groups[group])]
        swapping[group] = _swap_start(own_half, "halves_to_sibling_start_" + group)
        return own_half[-1]

    def update(group, after):
        sems_, own_half, lands_ = swapping[group]
        own_half, other_half = _swap_wait(sems_, own_half, lands_, after, "halves_to_sibling_wait_" + group)
        done = []
        for nm, own, oth in zip(groups[group], own_half, other_half):
            w_, m_, v_ = (view(a, nm) for a in weights[nm])
            res = _adamw_halves(c_arr, own, oth, w_, m_, v_, "adamw_" + nm)
            big[nm] = [r_.T[None] if nm in transposed else r_[None] for r_ in res]
            done.append(res[1])
        return done

    token_in = in_flight["in"][3]
    summed_ffn = sum_and_swap("ffn", [token_in])
    summed_mix = sum_and_swap("mix", [summed_ffn])

    total, g_b_mod, g_lb_full, g_w_mod, g_c_ctx = (small[k] for k in ("total", "g_b_mod", "g_lb_full", "g_w_mod",
                                                                      "g_c_ctx"))
    g_pre1, g_post1, g_pre2, g_post2 = (total[r_:r_ + 1] for r_ in (ROW_PRE1, ROW_POST1, ROW_PRE2, ROW_POST2))
    g_hg_on, g_gla_on = total[ROW_ONORM:ROW_ONORM + 1, 0:HD], total[ROW_ONORM:ROW_ONORM + 1, HD:2 * HD]
    n_lb = hg_lb.shape[2]
    g_hg_lb = lax.dynamic_slice(g_lb_full, (0, chip * n_lb), (4, n_lb))
    g_bgk = lax.dynamic_slice(total[ROW_BGK:ROW_BGK + 1].reshape(2, HW), (0, chip * n_lb), (2, n_lb))
    g_wgk_full = total[ROW_WGK:ROW_WGK + RANK].reshape(RANK, 2, HW).transpose(1, 0, 2).reshape(2 * RANK, HW)
    g_wgk = lax.dynamic_slice(g_wgk_full, (0, chip * n_lb), (2 * RANK, n_lb))

    small_items = [
        (g_c_ctx, c_ctx.reshape(1, d), m_c_ctx.reshape(1, d), v_c_ctx.reshape(1, d)),
        (g_b_mod, b_mod, m_b_mod, v_b_mod),
        (g_pre1, norm_pre1, m_norm_pre1, v_norm_pre1),
        (g_post1, norm_post1, m_norm_post1, v_norm_post1),
        (g_pre2, norm_pre2, m_norm_pre2, v_norm_pre2),
        (g_post2, norm_post2, m_norm_post2, v_norm_post2),
        (g_hg_lb, hg_lb.reshape(4, n_lb), m_hg_lb.reshape(4, n_lb), v_hg_lb.reshape(4, n_lb)),
        (g_hg_on, hg_onorm, m_hg_onorm, v_hg_onorm),
        (g_wgk, gla_w_gk.reshape(2 * RANK, n_lb), m_gla_w_gk.reshape(2 * RANK, n_lb), v_gla_w_gk.reshape(2 * RANK, n_lb)),
        (g_bgk, gla_b_gk.reshape(2, n_lb), m_gla_b_gk.reshape(2, n_lb), v_gla_b_gk.reshape(2, n_lb)),
        (g_gla_on, gla_onorm, m_gla_onorm, v_gla_onorm),
    ]
    small_res = _adamw_whole(small_items, "adamw_small")
    mod_res = _adamw_tiled(g_w_mod, w_mod[0], m_w_mod[0], v_w_mod[0], "adamw_w_mod")
    done_ffn = update("ffn", [summed_mix, mod_res[0], small_res[0][0]])
    done_mix = update("mix", done_ffn)
    update("in", [sum_and_swap("in", done_mix)])

    loss = total[ROW_LOSS, 0]

    shapes = dict(c_ctx=c_ctx.shape, b_mod=b_mod.shape, norm_pre1=norm_pre1.shape, norm_post1=norm_post1.shape,
                  norm_pre2=norm_pre2.shape, norm_post2=norm_post2.shape, hg_lb=hg_lb.shape, hg_onorm=hg_onorm.shape,
                  gla_w_gk=gla_w_gk.shape, gla_b_gk=gla_b_gk.shape, gla_onorm=gla_onorm.shape)
    small_names = ["c_ctx", "b_mod", "norm_pre1", "norm_post1", "norm_pre2", "norm_post2", "hg_lb", "hg_onorm",
                   "gla_w_gk", "gla_b_gk", "gla_onorm"]
    grads, deltas, new_m, new_v = {}, {}, {}, {}
    for nm, item, res in zip(small_names, small_items, small_res):
        grads[nm] = item[0].reshape(shapes[nm])
        deltas[nm], new_m[nm], new_v[nm] = (r_.reshape(shapes[nm]) for r_ in res)
    grads["w_mod"] = g_w_mod[None]
    deltas["w_mod"], new_m["w_mod"], new_v["w_mod"] = (r_[None] for r_ in mod_res)
    for nm in names:
        grads[nm], deltas[nm], new_m[nm], new_v[nm] = big[nm]
    order = ["c_ctx", "w_mod", "b_mod", "norm_pre1", "norm_post1", "norm_pre2", "norm_post2", "w_in", "hg_lb",
             "hg_onorm", "gla_w_gk", "gla_b_gk", "gla_onorm", "w_br_hg", "w_br_gla", "w_out", "w_ff_gate", "w_ff_up",
             "w_ff_down"]
    return (loss, grad_x[None], *[grads[n] for n in order], *[deltas[n] for n in order],
            *[new_m[n] for n in order], *[new_v[n] for n in order])
```

```python
import functools

import jax
import jax.numpy as jnp
from jax import lax
from jax.experimental import pallas as pl
from jax.experimental.pallas import tpu as pltpu

F32 = jnp.float32
BF16 = jnp.bfloat16
MESH = pl.DeviceIdType.MESH

EPS = 1e-6
CHUNK = 64
SUB = 16
NSUB = CHUNK // SUB
NH = 4
HD = 128
HW = NH * HD
RANK = 16
GATE_NORM = 16.0
N_MOD = 6
TM = 256
BWD_LANES = 8
N_DEV = 8
N_CHIP = 4
VMEM_LIMIT = 56 * 1024 * 1024

ADAM_LR = 0.001
ADAM_B1 = 0.9
ADAM_B2 = 0.999
ADAM_EPS = 1e-08
ADAM_WD = 0.01
ADAM_STEP = 10

VMEM_SPEC = pl.BlockSpec(memory_space=pltpu.VMEM)
ANY_SPEC = pl.BlockSpec(memory_space=pl.ANY)
HBM_SPEC = pl.BlockSpec(memory_space=pltpu.HBM)
SEM_SPEC = pl.BlockSpec(memory_space=pltpu.SEMAPHORE)
EFFECT = pltpu.SideEffectType.DATAFLOW_SIDE_EFFECTING


def _cparams(**kw):
    return pltpu.CompilerParams(vmem_limit_bytes=VMEM_LIMIT, **kw)


def _dot(a, b):
    return jnp.dot(a.astype(BF16), b.astype(BF16), preferred_element_type=F32)


def _dot_nt(a, b):
    return lax.dot_general(a.astype(BF16), b.astype(BF16), (((1,), (1,)), ((), ())), preferred_element_type=F32)


def _dot_tn(a, b):
    return lax.dot_general(a.astype(BF16), b.astype(BF16), (((0,), (0,)), ((), ())), preferred_element_type=F32)


def _sigmoid(x):
    return 1.0 / (1.0 + jnp.exp(-x))


def _silu(x):
    return x * _sigmoid(x)


def _dsilu(x):
    s = _sigmoid(x)
    return s * (1.0 + x * (1.0 - s))


def _log_sigmoid(x):
    return jnp.minimum(x, 0.0) - jnp.log(1.0 + jnp.exp(-jnp.abs(x)))


def _colsum(a):
    return jnp.sum(a, axis=0, keepdims=True)


def _rms(a):
    r = lax.rsqrt(jnp.mean(a * a, axis=-1, keepdims=True) + EPS)
    return a * r, r


def _rms_bwd(dn, n, r):
    return r * (dn - n * jnp.mean(dn * n, axis=-1, keepdims=True))


def _place():
    x, y, c = lax.axis_index("x"), lax.axis_index("y"), lax.axis_index("c")
    chips = [(1 - x, y), (x, 1 - y), (1 - x, 1 - y)]
    return x, y, c, chips


def _allgather8(v, name, after=()):
    rows, cols = v.shape
    n_after = len(after)

    def body(x_ref, *rest):
        out_ref, send_sems, recv_sems, local_sem = rest[n_after:]
        x, y, c, chips = _place()
        me, sibling = (x, y, c), (x, y, 1 - c)

        def blk(px, py, pc):
            return out_ref.at[4 * px + 2 * py + pc]

        def copy(k, block, to, src=None):
            return pltpu.make_async_remote_copy(
                src_ref=blk(*block) if src is None else src, dst_ref=blk(*block),
                send_sem=send_sems.at[k], recv_sem=recv_sems.at[k], device_id=to, device_id_type=MESH)

        mine = pltpu.make_async_copy(x_ref, blk(*me), local_sem)
        mine.start()
        first = [copy(0, me, sibling, src=x_ref)]
        first += [copy(1 + j, me, (*chip, c), src=x_ref) for j, chip in enumerate(chips)]
        for cp in first:
            cp.start()
        passed = [copy(4 + j, (*chip, c), sibling) for j, chip in enumerate(chips)]
        for j, chip in enumerate(chips):
            copy(1 + j, (*chip, c), me).wait_recv()
            passed[j].start()
        copy(0, sibling, me).wait_recv()
        for j, chip in enumerate(chips):
            copy(4 + j, (*chip, 1 - c), me).wait_recv()
        for cp in first + passed:
            cp.wait_send()
        mine.wait()

    return pl.pallas_call(
        body, name=name,
        out_shape=jax.ShapeDtypeStruct((N_DEV, rows, cols), v.dtype),
        in_specs=[VMEM_SPEC] + [ANY_SPEC] * n_after, out_specs=VMEM_SPEC,
        scratch_shapes=[pltpu.SemaphoreType.DMA((7,)), pltpu.SemaphoreType.DMA((7,)), pltpu.SemaphoreType.DMA],
    )(v, *after)


def _allgather8_start(v, name):
    rows, cols = v.shape

    def body(x_ref, out_ref, *rest):
        send_sems, recv_sems, token, local_sem = rest[:4], rest[4:8], rest[-2], rest[-1]
        x, y, c, chips = _place()
        own = out_ref.at[4 * x + 2 * y + c]
        mine = pltpu.make_async_copy(x_ref, own, local_sem)
        mine.start()
        mine.wait()
        for k, to in enumerate([(x, y, 1 - c)] + [(*chip, c) for chip in chips]):
            pltpu.make_async_remote_copy(src_ref=own, dst_ref=own, send_sem=send_sems[k], recv_sem=recv_sems[k],
                                         device_id=to, device_id_type=MESH).start()
        token[...] = jnp.zeros_like(token)

    land = _hbm(lax.empty((N_DEV, rows, cols), v.dtype))
    out = pl.pallas_call(
        body, name=name,
        out_shape=(*[pltpu.SemaphoreType.DMA(())] * 8, pltpu.HBM(land.shape, land.dtype),
                   jax.ShapeDtypeStruct((8, 128), F32)),
        in_specs=[VMEM_SPEC, HBM_SPEC],
        out_specs=(*[SEM_SPEC] * 8, HBM_SPEC, VMEM_SPEC),
        input_output_aliases={1: 8},
        scratch_shapes=[pltpu.SemaphoreType.DMA],
        compiler_params=pltpu.CompilerParams(has_side_effects=EFFECT),
    )(v, land)
    return list(out[:8]), out[8], out[9]


def _allgather8_wait(sems, land, after, name):
    def blk(ref, px, py, pc):
        return ref.at[4 * px + 2 * py + pc]

    def wait_body(out_ref, *rest):
        send_sems, recv_sems = rest[:4], rest[4:8]
        x, y, c, chips = _place()
        me = (x, y, c)
        for k, peer in enumerate([(x, y, 1 - c)] + [(*chip, c) for chip in chips]):
            sent = pltpu.make_async_remote_copy(
                src_ref=blk(out_ref, *me), dst_ref=blk(out_ref, *me), send_sem=send_sems[k], recv_sem=recv_sems[k],
                device_id=peer, device_id_type=MESH)
            sent.wait_send()
            pltpu.make_async_remote_copy(
                src_ref=blk(out_ref, *peer), dst_ref=blk(out_ref, *peer), send_sem=send_sems[k],
                recv_sem=recv_sems[k], device_id=me, device_id_type=MESH).wait_recv()

    def pass_body(out_ref, _, send_sems, recv_sems):
        x, y, c, chips = _place()
        started = []
        for j, chip in enumerate(chips):
            cp = pltpu.make_async_remote_copy(
                src_ref=blk(out_ref, *chip, c), dst_ref=blk(out_ref, *chip, c), send_sem=send_sems.at[j],
                recv_sem=recv_sems.at[j], device_id=(x, y, 1 - c), device_id_type=MESH)
            cp.start()
            started.append(cp)
        for j, chip in enumerate(chips):
            pltpu.make_async_remote_copy(
                src_ref=blk(out_ref, *chip, 1 - c), dst_ref=blk(out_ref, *chip, 1 - c), send_sem=send_sems.at[j],
                recv_sem=recv_sems.at[j], device_id=(x, y, c), device_id_type=MESH).wait_recv()
        for cp in started:
            cp.wait_send()

    land = pl.pallas_call(
        wait_body, name=name,
        out_shape=pltpu.HBM(land.shape, land.dtype),
        in_specs=[HBM_SPEC] + [SEM_SPEC] * 8 + [ANY_SPEC] * len(after),
        out_specs=HBM_SPEC,
        input_output_aliases={0: 0},
        compiler_params=pltpu.CompilerParams(has_side_effects=EFFECT),
    )(land, *sems, *after)
    return pl.pallas_call(
        pass_body, name=name + "_pass",
        out_shape=jax.ShapeDtypeStruct(land.shape, land.dtype),
        in_specs=[ANY_SPEC], out_specs=ANY_SPEC,
        input_output_aliases={0: 0},
        scratch_shapes=[pltpu.SemaphoreType.DMA((3,)), pltpu.SemaphoreType.DMA((3,))],
    )(land)


def _cast_into_blocks(chip_arr, w, name):
    rows, cols = w.shape
    tr = _row_tile(rows, 16, 256)

    def body(chip_ref, w_ref, o_ref):
        o_ref[0] = w_ref[...].astype(BF16)

    return pl.pallas_call(
        body, name=name,
        grid_spec=pltpu.PrefetchScalarGridSpec(
            num_scalar_prefetch=1, grid=(rows // tr,),
            in_specs=[pl.BlockSpec((tr, cols), lambda i, chip_ref: (i, 0))],
            out_specs=pl.BlockSpec((1, tr, cols), lambda i, chip_ref: (chip_ref[0], i, 0))),
        out_shape=jax.ShapeDtypeStruct((N_CHIP, rows, cols), BF16),
        compiler_params=_cparams(dimension_semantics=("parallel",)),
    )(chip_arr, w)


def _halved_by_rows(shape):
    return (shape[1] // 2) % 16 == 0


def _half_of(ref, pc, block=None):
    lead = slice(None) if block is None else block
    if _halved_by_rows(ref.shape):
        h = ref.shape[1] // 2
        return ref.at[lead, pl.ds(pl.multiple_of(pc * h, 16), h), :]
    h = ref.shape[2] // 2
    return ref.at[lead, :, pl.ds(pl.multiple_of(pc * h, 128), h)]


def _half_shape(shape):
    return (shape[0], shape[1] // 2, shape[2]) if _halved_by_rows(shape) else (shape[0], shape[1], shape[2] // 2)


def _half_rows(ref, chip_id, pc):
    return _half_of(ref, pc, chip_id)


def _hbm(a):
    return pltpu.with_memory_space_constraint(a, pltpu.HBM)


def _blocks_start(lands, name, after=()):
    n = len(lands)
    n_sem = 3 * n
    first = n + len(after)

    def body(*refs):
        lnd = refs[:n]
        send_sems, recv_sems = refs[first:first + n_sem], refs[first + n_sem:first + 2 * n_sem]
        token = refs[-1]
        x, y, c, chips = _place()
        me_chip = 2 * x + y
        for k in range(n):
            for j, chip in enumerate(chips):
                pltpu.make_async_remote_copy(
                    src_ref=_half_rows(lnd[k], me_chip, c), dst_ref=_half_rows(lnd[k], me_chip, c),
                    send_sem=send_sems[3 * k + j], recv_sem=recv_sems[3 * k + j],
                    device_id=(*chip, c), device_id_type=MESH).start()
        token[...] = jnp.zeros_like(token)

    out = pl.pallas_call(
        body, name=name,
        out_shape=(*[pltpu.SemaphoreType.DMA(())] * (2 * n_sem),
                   *[pltpu.HBM(l.shape, l.dtype) for l in lands],
                   jax.ShapeDtypeStruct((8, 128), F32)),
        in_specs=[HBM_SPEC] * n + [ANY_SPEC] * len(after),
        out_specs=(*[SEM_SPEC] * (2 * n_sem), *[HBM_SPEC] * n, VMEM_SPEC),
        input_output_aliases={i: 2 * n_sem + i for i in range(n)},
        compiler_params=pltpu.CompilerParams(has_side_effects=EFFECT),
    )(*[_hbm(l) for l in lands], *after)
    return list(out[:2 * n_sem]), list(out[2 * n_sem:2 * n_sem + n]), out[-1]


def _blocks_wait(sems, lands, after, name):
    n = len(lands)
    n_sem = 3 * n

    def body(*refs):
        lnd = refs[:n]
        s_sems, r_sems = refs[n:n + n_sem], refs[n + n_sem:n + 2 * n_sem]
        x, y, c, chips = _place()
        me_chip = 2 * x + y
        for k in range(n):
            for j, (px, py) in enumerate(chips):
                cp = pltpu.make_async_remote_copy(
                    src_ref=_half_rows(lnd[k], me_chip, c), dst_ref=_half_rows(lnd[k], 2 * px + py, c),
                    send_sem=s_sems[3 * k + j], recv_sem=r_sems[3 * k + j],
                    device_id=(px, py, c), device_id_type=MESH)
                cp.wait_send()
                cp.wait_recv()

    out = pl.pallas_call(
        body, name=name,
        out_shape=tuple(pltpu.HBM(l.shape, l.dtype) for l in lands),
        in_specs=[HBM_SPEC] * n + [SEM_SPEC] * (2 * n_sem) + [ANY_SPEC] * len(after),
        out_specs=[HBM_SPEC] * n,
        input_output_aliases={i: i for i in range(n)},
        compiler_params=pltpu.CompilerParams(has_side_effects=EFFECT),
    )(*lands, *sems, *after)
    return list(out)


def _forward_start(lands, name):
    n = len(lands)
    n_sem = 3 * n

    def body(*refs):
        lnd = refs[:n]
        send_sems, recv_sems = refs[n:n + n_sem], refs[n + n_sem:n + 2 * n_sem]
        x, y, c, chips = _place()
        for k in range(n):
            for j, (px, py) in enumerate(chips):
                pltpu.make_async_remote_copy(
                    src_ref=_half_rows(lnd[k], 2 * px + py, c), dst_ref=_half_rows(lnd[k], 2 * px + py, c),
                    send_sem=send_sems[3 * k + j], recv_sem=recv_sems[3 * k + j],
                    device_id=(x, y, 1 - c), device_id_type=MESH).start()
        refs[-1][...] = jnp.zeros_like(refs[-1])

    out = pl.pallas_call(
        body, name=name,
        out_shape=(*[pltpu.SemaphoreType.DMA(())] * (2 * n_sem), *[pltpu.HBM(l.shape, l.dtype) for l in lands],
                   jax.ShapeDtypeStruct((8, 128), F32)),
        in_specs=[HBM_SPEC] * n,
        out_specs=(*[SEM_SPEC] * (2 * n_sem), *[HBM_SPEC] * n, VMEM_SPEC),
        input_output_aliases={i: 2 * n_sem + i for i in range(n)},
        compiler_params=pltpu.CompilerParams(has_side_effects=EFFECT),
    )(*[_hbm(l) for l in lands])
    return list(out[:2 * n_sem]), list(out[2 * n_sem:2 * n_sem + n]), out[-1]


def _forward_wait(sems, lands, after, name):
    n = len(lands)
    n_sem = 3 * n

    def body(*refs):
        lnd = refs[:n]
        s_sems, r_sems = refs[n:n + n_sem], refs[n + n_sem:n + 2 * n_sem]
        x, y, c, chips = _place()
        for k in range(n):
            for j, (px, py) in enumerate(chips):
                cp = pltpu.make_async_remote_copy(
                    src_ref=_half_rows(lnd[k], 2 * px + py, c), dst_ref=_half_rows(lnd[k], 2 * px + py, 1 - c),
                    send_sem=s_sems[3 * k + j], recv_sem=r_sems[3 * k + j],
                    device_id=(x, y, 1 - c), device_id_type=MESH)
                cp.wait_send()
                cp.wait_recv()

    out = pl.pallas_call(
        body, name=name,
        out_shape=tuple(pltpu.HBM(l.shape, l.dtype) for l in lands),
        in_specs=[HBM_SPEC] * n + [SEM_SPEC] * (2 * n_sem) + [ANY_SPEC] * len(after),
        out_specs=[HBM_SPEC] * n,
        input_output_aliases={i: i for i in range(n)},
        compiler_params=pltpu.CompilerParams(has_side_effects=EFFECT),
    )(*lands, *sems, *after)
    return list(out)


def _blocks_finish(lands, name):
    n = len(lands)

    def body(*refs):
        lnd = refs[n:2 * n]
        send_sems, recv_sems = refs[2 * n:]
        x, y, c, chips = _place()
        sibling = (x, y, 1 - c)

        def copy(k, j, chip_id, pc):
            return pltpu.make_async_remote_copy(
                src_ref=_half_rows(lnd[k], chip_id, pc), dst_ref=_half_rows(lnd[k], chip_id, pc),
                send_sem=send_sems.at[k, j], recv_sem=recv_sems.at[k, j], device_id=sibling, device_id_type=MESH)

        started = []
        for k in range(n):
            for j, (px, py) in enumerate(chips):
                cp = copy(k, j, 2 * px + py, c)
                cp.start()
                started.append(cp)
        for k in range(n):
            for j, (px, py) in enumerate(chips):
                copy(k, j, 2 * px + py, 1 - c).wait_recv()
        for cp in started:
            cp.wait_send()

    out = pl.pallas_call(
        body, name=name,
        out_shape=[jax.ShapeDtypeStruct(l.shape, l.dtype) for l in lands],
        in_specs=[ANY_SPEC] * n, out_specs=[ANY_SPEC] * n,
        input_output_aliases={i: i for i in range(n)},
        scratch_shapes=[pltpu.SemaphoreType.DMA((n, 3)), pltpu.SemaphoreType.DMA((n, 3))],
    )(*lands)
    return list(out)


def _send_half_start(arrs, name, after=()):
    n = len(arrs)
    first = 2 * n + len(after)

    def body(*refs):
        ins, lnd = refs[:n], refs[n:2 * n]
        send_sems, recv_sems = refs[first:first + n], refs[first + n:first + 2 * n]
        token = refs[-1]
        x, y, c, _ = _place()
        for k in range(n):
            pltpu.make_async_remote_copy(
                src_ref=_half_of(ins[k], 1 - c), dst_ref=lnd[k], send_sem=send_sems[k], recv_sem=recv_sems[k],
                device_id=(x, y, 1 - c), device_id_type=MESH).start()
        token[...] = jnp.zeros_like(token)

    lands = [_hbm(lax.empty(_half_shape(a.shape), a.dtype)) for a in arrs]
    out = pl.pallas_call(
        body, name=name,
        out_shape=(*[pltpu.SemaphoreType.DMA(())] * (2 * n), *[pltpu.HBM(a.shape, a.dtype) for a in arrs],
                   *[pltpu.HBM(l.shape, l.dtype) for l in lands], jax.ShapeDtypeStruct((8, 128), F32)),
        in_specs=[HBM_SPEC] * (2 * n) + [ANY_SPEC] * len(after),
        out_specs=(*[SEM_SPEC] * (2 * n), *[HBM_SPEC] * (2 * n), VMEM_SPEC),
        input_output_aliases={i: 2 * n + i for i in range(2 * n)},
        compiler_params=pltpu.CompilerParams(has_side_effects=EFFECT),
    )(*[_hbm(a) for a in arrs], *lands, *after)
    return list(out[:2 * n]), list(out[2 * n:3 * n]), list(out[3 * n:4 * n]), out[-1]


def _send_half_wait(sems, arrs, lands, after, name):
    n = len(arrs)

    def body(*refs):
        ins, lnd = refs[:n], refs[n:2 * n]
        s_sems, r_sems = refs[2 * n:3 * n], refs[3 * n:4 * n]
        x, y, c, _ = _place()
        for k in range(n):
            cp = pltpu.make_async_remote_copy(
                src_ref=_half_of(ins[k], 1 - c), dst_ref=lnd[k], send_sem=s_sems[k], recv_sem=r_sems[k],
                device_id=(x, y, 1 - c), device_id_type=MESH)
            cp.wait_send()
            cp.wait_recv()

    out = pl.pallas_call(
        body, name=name,
        out_shape=tuple(pltpu.HBM(a.shape, a.dtype) for a in list(arrs) + list(lands)),
        in_specs=[HBM_SPEC] * (2 * n) + [SEM_SPEC] * (2 * n) + [ANY_SPEC] * len(after),
        out_specs=[HBM_SPEC] * (2 * n),
        input_output_aliases={i: i for i in range(2 * n)},
        compiler_params=pltpu.CompilerParams(has_side_effects=EFFECT),
    )(*arrs, *lands, *sems, *after)
    return list(out[:n]), list(out[n:])


def _scatter_start(arrs, name, after=()):
    n = len(arrs)
    n_sem = 3 * n
    first = 2 * n + len(after)

    def body(*refs):
        ins, lnd = refs[:n], refs[n:2 * n]
        send_sems, recv_sems = refs[first:first + n_sem], refs[first + n_sem:first + 2 * n_sem]
        token = refs[-1]
        x, y, c, chips = _place()
        me_chip = 2 * x + y
        for k in range(n):
            for j, (px, py) in enumerate(chips):
                pltpu.make_async_remote_copy(
                    src_ref=ins[k].at[2 * px + py], dst_ref=lnd[k].at[me_chip],
                    send_sem=send_sems[3 * k + j], recv_sem=recv_sems[3 * k + j],
                    device_id=(px, py, c), device_id_type=MESH).start()
        token[...] = jnp.zeros_like(token)

    lands = [_hbm(lax.empty(a.shape, a.dtype)) for a in arrs]
    out = pl.pallas_call(
        body, name=name,
        out_shape=(*[pltpu.SemaphoreType.DMA(())] * (2 * n_sem),
                   *[pltpu.HBM(a.shape, a.dtype) for a in arrs], *[pltpu.HBM(a.shape, a.dtype) for a in arrs],
                   jax.ShapeDtypeStruct((8, 128), F32)),
        in_specs=[HBM_SPEC] * (2 * n) + [ANY_SPEC] * len(after),
        out_specs=(*[SEM_SPEC] * (2 * n_sem), *[HBM_SPEC] * (2 * n), VMEM_SPEC),
        input_output_aliases={i: 2 * n_sem + i for i in range(2 * n)},
        compiler_params=pltpu.CompilerParams(has_side_effects=EFFECT),
    )(*[_hbm(a) for a in arrs], *lands, *after)
    base = 2 * n_sem
    return list(out[:base]), list(out[base:base + n]), list(out[base + n:base + 2 * n]), out[-1]


def _scatter_wait(sems, arrs, lands, after, name):
    n = len(arrs)
    n_sem = 3 * n

    def body(*refs):
        ins, lnd = refs[:n], refs[n:2 * n]
        s_sems, r_sems = refs[2 * n:2 * n + n_sem], refs[2 * n + n_sem:2 * n + 2 * n_sem]
        x, y, c, chips = _place()
        for k in range(n):
            for j, (px, py) in enumerate(chips):
                cp = pltpu.make_async_remote_copy(
                    src_ref=ins[k].at[2 * px + py], dst_ref=lnd[k].at[2 * px + py],
                    send_sem=s_sems[3 * k + j], recv_sem=r_sems[3 * k + j],
                    device_id=(px, py, c), device_id_type=MESH)
                cp.wait_send()
                cp.wait_recv()

    out = pl.pallas_call(
        body, name=name,
        out_shape=tuple(pltpu.HBM(a.shape, a.dtype) for a in list(arrs) + list(lands)),
        in_specs=[HBM_SPEC] * (2 * n) + [SEM_SPEC] * (2 * n_sem) + [ANY_SPEC] * len(after),
        out_specs=[HBM_SPEC] * (2 * n),
        input_output_aliases={i: i for i in range(2 * n)},
        compiler_params=pltpu.CompilerParams(has_side_effects=EFFECT),
    )(*arrs, *lands, *sems, *after)
    return list(out[:n]), list(out[n:])


def _sum_owner(chip_arr, pairs, got, name):
    nb, h, cols = got.shape
    tr = _row_tile(h, 16, 256)

    def body(chip_ref, own_ref, a_ref, b_ref, c_ref, o_ref):
        o_ref[...] = ((own_ref[0].astype(F32) + a_ref[0].astype(F32)) + b_ref[0].astype(F32)) + c_ref[0].astype(F32)

    def slot(off):
        return pl.BlockSpec((1, tr, cols), lambda i, chip_ref: ((chip_ref[0] + off) % N_CHIP, i, 0))

    return pl.pallas_call(
        body, name=name,
        grid_spec=pltpu.PrefetchScalarGridSpec(
            num_scalar_prefetch=1, grid=(h // tr,),
            in_specs=[slot(0), slot(1), slot(2), slot(3)],
            out_specs=pl.BlockSpec((tr, cols), lambda i, chip_ref: (i, 0))),
        out_shape=jax.ShapeDtypeStruct((h, cols), F32),
        compiler_params=_cparams(dimension_semantics=("parallel",)),
    )(chip_arr, pairs, got, got, got)


def _swap_start(arrs, name, after=()):
    n = len(arrs)
    first = 2 * n + len(after)

    def body(*refs):
        ins, lnd = refs[:n], refs[n:2 * n]
        send_sems, recv_sems = refs[first:first + n], refs[first + n:first + 2 * n]
        x, y, c, _ = _place()
        for k in range(n):
            pltpu.make_async_remote_copy(
                src_ref=ins[k], dst_ref=lnd[k], send_sem=send_sems[k], recv_sem=recv_sems[k],
                device_id=(x, y, 1 - c), device_id_type=MESH).start()

    lands = [_hbm(lax.empty(a.shape, a.dtype)) for a in arrs]
    out = pl.pallas_call(
        body, name=name,
        out_shape=(*[pltpu.SemaphoreType.DMA(())] * (2 * n), *[pltpu.HBM(a.shape, a.dtype) for a in arrs],
                   *[pltpu.HBM(a.shape, a.dtype) for a in arrs]),
        in_specs=[HBM_SPEC] * (2 * n) + [ANY_SPEC] * len(after),
        out_specs=(*[SEM_SPEC] * (2 * n), *[HBM_SPEC] * (2 * n)),
        input_output_aliases={i: 2 * n + i for i in range(2 * n)},
        compiler_params=pltpu.CompilerParams(has_side_effects=EFFECT),
    )(*[_hbm(a) for a in arrs], *lands, *after)
    return list(out[:2 * n]), list(out[2 * n:3 * n]), list(out[3 * n:4 * n])


def _swap_wait(sems, arrs, lands, after, name):
    n = len(arrs)

    def body(*refs):
        ins, lnd = refs[:n], refs[n:2 * n]
        s_sems, r_sems = refs[2 * n:3 * n], refs[3 * n:4 * n]
        x, y, c, _ = _place()
        for k in range(n):
            cp = pltpu.make_async_remote_copy(
                src_ref=ins[k], dst_ref=lnd[k], send_sem=s_sems[k], recv_sem=r_sems[k],
                device_id=(x, y, 1 - c), device_id_type=MESH)
            cp.wait_send()
            cp.wait_recv()

    out = pl.pallas_call(
        body, name=name,
        out_shape=tuple(pltpu.HBM(a.shape, a.dtype) for a in list(arrs) + list(lands)),
        in_specs=[HBM_SPEC] * (2 * n) + [SEM_SPEC] * (2 * n) + [ANY_SPEC] * len(after),
        out_specs=[HBM_SPEC] * (2 * n),
        input_output_aliases={i: i for i in range(2 * n)},
        compiler_params=pltpu.CompilerParams(has_side_effects=EFFECT),
    )(*arrs, *lands, *sems, *after)
    return list(out[:n]), list(out[n:])


def _row_tile(h, mult=8, cap=128):
    for t in range(cap - cap % mult, mult - 1, -mult):
        if h % t == 0:
            return t
    if mult > 8:
        return _row_tile(h, 8, cap)
    raise ValueError(h)


def _pair_sum(c_arr, full, recv, name):
    nb, rows, cols = full.shape

    def body(c_ref, f_ref, r_ref, o_ref):
        o_ref[...] = (f_ref[...] + r_ref[...]).astype(BF16)

    if _halved_by_rows(full.shape):
        h = rows // 2
        tr = _row_tile(h, 16, 256)
        steps = h // tr
        own = pl.BlockSpec((1, tr, cols), lambda b, i, c_ref: (b, c_ref[0] * steps + i, 0))
        half = pl.BlockSpec((1, tr, cols), lambda b, i, c_ref: (b, i, 0))
    else:
        steps = 1
        own = pl.BlockSpec((1, rows, cols // 2), lambda b, i, c_ref: (b, 0, c_ref[0]))
        half = pl.BlockSpec((1, rows, cols // 2), lambda b, i, c_ref: (b, 0, 0))
    return pl.pallas_call(
        body, name=name,
        grid_spec=pltpu.PrefetchScalarGridSpec(
            num_scalar_prefetch=1, grid=(nb, steps), in_specs=[own, half], out_specs=half),
        out_shape=jax.ShapeDtypeStruct(_half_shape(full.shape), BF16),
        compiler_params=_cparams(dimension_semantics=("parallel", "parallel")),
    )(c_arr, full, recv)


def _adam_math(g, w, m, v):
    m1 = ADAM_B1 * m + (1.0 - ADAM_B1) * g
    v1 = ADAM_B2 * v + (1.0 - ADAM_B2) * (g * g)
    m_hat = m1 / (1.0 - ADAM_B1 ** ADAM_STEP)
    v_hat = v1 / (1.0 - ADAM_B2 ** ADAM_STEP)
    delta = -ADAM_LR * (m_hat / (jnp.sqrt(v_hat) + ADAM_EPS) + ADAM_WD * w)
    return delta, m1, v1


def _adamw_halves(c_arr, own, other, w, m, v, name):
    rows, cols = w.shape
    by_rows = own.shape[1] == cols

    def body(c_ref, own_ref, oth_ref, w_ref, m_ref, v_ref, g_out, d_out, m_out, v_out):
        if by_rows:
            g = jnp.where(pl.program_id(0) == c_ref[0], own_ref[...], oth_ref[...])
        else:
            own_, oth_ = own_ref[...], oth_ref[...]
            g = jnp.where(c_ref[0] == 0, jnp.concatenate([own_, oth_], axis=1), jnp.concatenate([oth_, own_], axis=1))
        d, m1, v1 = _adam_math(g, w_ref[...], m_ref[...], v_ref[...])
        g_out[...] = g
        d_out[...] = d
        m_out[...] = m1
        v_out[...] = v1

    if by_rows:
        h = rows // 2
        tr = _row_tile(h)
        steps = h // tr
        grid = (2, steps)
        half_spec = pl.BlockSpec((tr, cols), lambda p, i, c_ref: (i, 0))
        full_spec = pl.BlockSpec((tr, cols), lambda p, i, c_ref: (p * steps + i, 0))
    else:
        tr = _row_tile(rows)
        grid = (1, rows // tr)
        half_spec = pl.BlockSpec((tr, cols // 2), lambda p, i, c_ref: (i, 0))
        full_spec = pl.BlockSpec((tr, cols), lambda p, i, c_ref: (i, 0))
    return pl.pallas_call(
        body, name=name,
        grid_spec=pltpu.PrefetchScalarGridSpec(
            num_scalar_prefetch=1, grid=grid,
            in_specs=[half_spec, half_spec, full_spec, full_spec, full_spec],
            out_specs=[full_spec] * 4),
        out_shape=[jax.ShapeDtypeStruct(w.shape, F32)] * 4,
        compiler_params=_cparams(dimension_semantics=("parallel", "parallel")),
    )(c_arr, own, other, w, m, v)


def _adamw_whole(items, name):
    n = len(items)

    def body(*refs):
        ins, outs = refs[:4 * n], refs[4 * n:]
        for k in range(n):
            g, w, m, v = (r[...] for r in ins[4 * k:4 * k + 4])
            d, m1, v1 = _adam_math(g, w, m, v)
            outs[3 * k][...] = d
            outs[3 * k + 1][...] = m1
            outs[3 * k + 2][...] = v1

    flat = [a for it in items for a in it]
    shapes = [jax.ShapeDtypeStruct(it[1].shape, F32) for it in items for _ in range(3)]
    out = pl.pallas_call(
        body, name=name, out_shape=shapes,
        in_specs=[VMEM_SPEC] * (4 * n), out_specs=[VMEM_SPEC] * (3 * n),
        compiler_params=_cparams(),
    )(*flat)
    return [tuple(out[3 * k:3 * k + 3]) for k in range(n)]


def _adamw_tiled(g, w, m, v, name):
    rows, cols = w.shape
    tr = _row_tile(rows)

    def body(g_ref, w_ref, m_ref, v_ref, d_out, m_out, v_out):
        d, m1, v1 = _adam_math(g_ref[...], w_ref[...], m_ref[...], v_ref[...])
        d_out[...] = d
        m_out[...] = m1
        v_out[...] = v1

    spec = pl.BlockSpec((tr, cols), lambda i: (i, 0))
    return pl.pallas_call(
        body, name=name, grid=(rows // tr,),
        out_shape=[jax.ShapeDtypeStruct(w.shape, F32)] * 3,
        in_specs=[spec] * 4, out_specs=[spec] * 3,
        compiler_params=_cparams(dimension_semantics=("parallel",)),
    )(g, w, m, v)


def _mod_forward(cond, w_mod, b_mod_cols, name):
    def body(c_ref, w_ref, b_ref, o_ref):
        o_ref[...] = _dot(_silu(c_ref[...]), w_ref[...]) + b_ref[...]

    return pl.pallas_call(
        body, name=name, out_shape=jax.ShapeDtypeStruct((cond.shape[0], w_mod.shape[1]), F32),
        in_specs=[VMEM_SPEC] * 3, out_specs=VMEM_SPEC, compiler_params=_cparams(),
    )(cond, w_mod, b_mod_cols)


def _mod_backward(cond, w_mod, dmod_cols, name):
    def body(c_ref, w_ref, d_ref, gw_ref, gc_ref):
        s = _silu(c_ref[...])
        d = d_ref[...]
        gw_ref[...] = _dot_tn(s, d)
        gc_ref[...] = _dot_nt(d[8:16, :], w_ref[...])

    return pl.pallas_call(
        body, name=name,
        out_shape=[jax.ShapeDtypeStruct(w_mod.shape, F32), jax.ShapeDtypeStruct((8, w_mod.shape[0]), F32)],
        in_specs=[VMEM_SPEC] * 3, out_specs=[VMEM_SPEC] * 2, compiler_params=_cparams(),
    )(cond, w_mod, dmod_cols)


def _col_chunks(width, step=512):
    return [(s, min(step, width - s)) for s in range(0, width, step)]


def _w_in_row(p_off):
    if p_off < 9 * HW:
        return p_off
    return 9 * HW if p_off == OFF_LR else p_off + 2 * RANK


def _in_projection(ctx0, x0, modc, modx, pre1, w_t, n_ctx_tiles, name):
    d = x0.shape[1]
    rows = ctx0.shape[0] + x0.shape[0]
    width = P_WIDTH

    def body(ctx_ref, x_ref, modc_ref, modx_ref, pre_ref, w_ref, h_ref, p_ref):
        is_ctx = pl.program_id(0) < n_ctx_tiles
        n, _ = _rms(jnp.where(is_ctx, ctx_ref[...], x_ref[...]))
        shift = jnp.where(is_ctx, modc_ref[0:1, :], modx_ref[0:1, :])
        scale = jnp.where(is_ctx, modc_ref[1:2, :], modx_ref[1:2, :])
        h = (n * pre_ref[...] * (1.0 + scale) + shift).astype(BF16)
        h_ref[...] = h
        for s, w in _col_chunks(width):
            p_ref[:, s:s + w] = _dot_nt(h, w_ref[_w_in_row(s):_w_in_row(s) + w, :])

    row = lambda i: (i, 0)
    fixed = lambda i: (0, 0)
    return pl.pallas_call(
        body, name=name, grid=(rows // TM,),
        out_shape=[jax.ShapeDtypeStruct((rows, d), BF16), jax.ShapeDtypeStruct((rows, width), F32)],
        in_specs=[pl.BlockSpec((TM, d), lambda i: (jnp.minimum(i, n_ctx_tiles - 1), 0)),
                  pl.BlockSpec((TM, d), lambda i: (jnp.maximum(i - n_ctx_tiles, 0), 0)),
                  pl.BlockSpec((8, d), fixed), pl.BlockSpec((8, d), fixed), pl.BlockSpec((1, d), fixed), VMEM_SPEC],
        out_specs=[pl.BlockSpec((TM, d), row), pl.BlockSpec((TM, width), row)],
        compiler_params=_cparams(dimension_semantics=("parallel",)),
    )(ctx0, x0, modc, modx, pre1, w_t)


C_HQ, C_HI, C_HF_FW, C_HF_BW, C_HGATE, C_GQ, C_GK, C_GV, C_GGATE = range(9)
OFF_GATE_HG = 9 * HW
OFF_LR = 13 * HW
P_WIDTH = OFF_LR + 128


def _head_norm_fwd(o, w):
    outs, ns, rs = [], [], []
    for h in range(NH):
        n, r = _rms(o[:, h * HD:(h + 1) * HD])
        ns.append(n)
        rs.append(r)
        outs.append(n * w)
    return jnp.concatenate(outs, axis=1), ns, rs


def _mixer_tail(z, o_hg, o_gla, p_hgate, p_ggate, p_gate_hg, p_gate_gla, hg_on, gla_on, wbh, wbg, wout):
    on_hg, n_hg, r_hg = _head_norm_fwd(o_hg, hg_on)
    on_gla, n_gla, r_gla = _head_norm_fwd(o_gla, gla_on)
    og_hg = (on_hg * _silu(p_hgate)).astype(BF16)
    og_gla = (on_gla * _silu(p_ggate)).astype(BF16)
    b_hg = jnp.dot(og_hg, wbh, preferred_element_type=F32)
    b_gla = jnp.dot(og_gla, wbg, preferred_element_type=F32)
    s_hg = _sigmoid(p_gate_hg)
    s_gla = _sigmoid(p_gate_gla)
    merged = (s_hg * b_hg + s_gla * b_gla).astype(BF16)
    y1 = jnp.dot(merged, wout, preferred_element_type=F32)
    return dict(on_hg=on_hg, n_hg=n_hg, r_hg=r_hg, on_gla=on_gla, n_gla=n_gla, r_gla=r_gla, og_hg=og_hg,
                og_gla=og_gla, b_hg=b_hg, b_gla=b_gla, s_hg=s_hg, s_gla=s_gla, merged=merged, y1=y1)


def _mixer_tail_fwd(x_lat, p, o_list, modx, norms, onorms, w_br_hg, w_br_gla, w_out, n_ctx_tiles, name):
    rows, d = x_lat.shape

    def body(x_ref, ofw_hg, obw_hg, ofw_gla, obw_gla, p_hgate, p_ggate, p_ghg_a, p_ghg_b, p_ggla_a, p_ggla_b,
             modx_ref, norm_ref, on_ref, wbh_ref, wbg_ref, wout_ref, z2_ref, y1_ref, mrg_ref, oghg_ref, oggla_ref):
        p_gate_hg = jnp.concatenate([p_ghg_a[...], p_ghg_b[...]], axis=1)
        p_gate_gla = jnp.concatenate([p_ggla_a[...], p_ggla_b[...]], axis=1)
        t = _mixer_tail(x_ref[...], ofw_hg[...] + obw_hg[...], ofw_gla[...] + obw_gla[...], p_hgate[...],
                        p_ggate[...], p_gate_hg, p_gate_gla, on_ref[0:1, 0:HD], on_ref[1:2, 0:HD],
                        wbh_ref[...], wbg_ref[...], wout_ref[...])
        y1_ref[...] = t["y1"]
        mrg_ref[...] = t["merged"]
        oghg_ref[...] = t["og_hg"]
        oggla_ref[...] = t["og_gla"]
        n1, _ = _rms(t["y1"])
        z2_ref[...] = x_ref[...] + n1 * norm_ref[1:2, :] * modx_ref[2:3, :]

    lat = lambda i: (i, 0)
    full = lambda i: (i + n_ctx_tiles, 0)
    fixed = lambda i: (0, 0)

    def pcol(blk):
        return pl.BlockSpec((TM, HW), lambda i: (i + n_ctx_tiles, blk))

    in_specs = ([pl.BlockSpec((TM, d), lat)] + [pl.BlockSpec((TM, HW), full)] * 4
                + [pcol(C_HGATE), pcol(C_GGATE), pcol(9), pcol(10), pcol(11), pcol(12)]
                + [pl.BlockSpec((8, d), fixed)] * 3 + [VMEM_SPEC] * 3)
    bf = lambda w: jax.ShapeDtypeStruct((rows, w), BF16)
    f32 = jax.ShapeDtypeStruct((rows, d), F32)
    return pl.pallas_call(
        body, name=name, grid=(rows // TM,), out_shape=[f32, f32, bf(d), bf(HW), bf(HW)], in_specs=in_specs,
        out_specs=[pl.BlockSpec((TM, d), lat)] * 3 + [pl.BlockSpec((TM, HW), lat)] * 2,
        compiler_params=_cparams(dimension_semantics=("parallel",)),
    )(x_lat, *o_list, p, p, p, p, p, p, modx, norms, onorms, w_br_hg, w_br_gla, w_out)


def _ffn_fwd_bwd(z2, modx, norms, w_gate, w_up, w_down, target, name):
    rows, d = z2.shape
    dff = w_gate.shape[0]
    inv_d = 1.0 / d

    def body(z2_ref, modx_ref, norm_ref, wg_ref, wu_ref, wd_ref, t_ref,
             loss_ref, dz2_ref, h2_ref, a_ref, du_ref, dv_ref, dy2_ref, stat_ref):
        i = pl.program_id(0)
        pre2, post2 = norm_ref[2:3, :], norm_ref[3:4, :]
        shift2, scale2, gate2 = modx_ref[3:4, :], modx_ref[4:5, :], modx_ref[5:6, :]
        z2 = z2_ref[...]
        n2, r2 = _rms(z2)
        nw2 = n2 * pre2
        h2 = (nw2 * (1.0 + scale2) + shift2).astype(BF16)
        h2_ref[...] = h2
        u = _dot_nt(h2, wg_ref[...])
        v = _dot_nt(h2, wu_ref[...])
        su = _silu(u)
        a = (su * v).astype(BF16)
        a_ref[...] = a
        y2 = jnp.dot(a, wd_ref[...], preferred_element_type=F32)
        n3, r3 = _rms(y2)
        err = z2 + n3 * post2 * gate2 - t_ref[...]
        part = 0.5 * inv_d * jnp.sum(err * err)
        dz3 = err * inv_d
        dgate2 = _colsum(dz3 * n3 * post2)
        tt = dz3 * gate2
        dpost2 = _colsum(tt * n3)
        dy2 = _rms_bwd(tt * post2, n3, r3).astype(BF16)
        dy2_ref[...] = dy2
        da = _dot_nt(dy2, wd_ref[...])
        du = (da * v * _dsilu(u)).astype(BF16)
        dv = (da * su).astype(BF16)
        du_ref[...] = du
        dv_ref[...] = dv
        dh2 = (jnp.dot(du, wg_ref[...], preferred_element_type=F32)
               + jnp.dot(dv, wu_ref[...], preferred_element_type=F32))
        dshift2 = _colsum(dh2)
        dscale2 = _colsum(dh2 * nw2)
        dnw2 = dh2 * (1.0 + scale2)
        dpre2 = _colsum(dnw2 * n2)
        dz2_ref[...] = dz3 + _rms_bwd(dnw2 * pre2, n2, r2)

        @pl.when(i == 0)
        def _():
            stat_ref[...] = jnp.zeros_like(stat_ref)
            loss_ref[...] = jnp.zeros_like(loss_ref)

        for r, val in enumerate((dshift2, dscale2, dgate2, dpre2, dpost2)):
            stat_ref[r:r + 1, :] += val
        loss_ref[...] += part
        stat_ref[5:6, 0:128] += part

    lat = lambda i: (i, 0)
    fixed = lambda i: (0, 0)
    bf = lambda w: jax.ShapeDtypeStruct((rows, w), BF16)
    return pl.pallas_call(
        body, name=name, grid=(rows // TM,),
        out_shape=[jax.ShapeDtypeStruct((8, 128), F32), jax.ShapeDtypeStruct((rows, d), F32), bf(d), bf(dff), bf(dff),
                   bf(dff), bf(d), jax.ShapeDtypeStruct((8, d), F32)],
        in_specs=[pl.BlockSpec((TM, d), lat), pl.BlockSpec((8, d), fixed), pl.BlockSpec((8, d), fixed)]
        + [VMEM_SPEC] * 3 + [pl.BlockSpec((TM, d), lat)],
        out_specs=[pl.BlockSpec((8, 128), fixed), pl.BlockSpec((TM, d), lat), pl.BlockSpec((TM, d), lat),
                   pl.BlockSpec((TM, dff), lat), pl.BlockSpec((TM, dff), lat), pl.BlockSpec((TM, dff), lat),
                   pl.BlockSpec((TM, d), lat), pl.BlockSpec((8, d), fixed)],
        compiler_params=_cparams(dimension_semantics=("arbitrary",)),
    )(z2, modx, norms, w_gate, w_up, w_down, target)


def _mixer_tail_bwd(x_lat, p, o_list, dz2, y1, modx, norms, onorms, w_br_hg, w_br_gla, w_out, n_ctx_tiles, n_tiles,
                    name):
    rows, d = x_lat.shape
    total = n_tiles * TM

    def body(x_ref, ofw_hg, obw_hg, ofw_gla, obw_gla, p_hgate, p_ggate, p_ghg_a, p_ghg_b, p_ggla_a, p_ggla_b,
             dz2_ref, y1_ref, modx_ref, norm_ref, on_ref, wbh_ref, wbg_ref, wout_ref,
             dohg_ref, dogla_ref, dhgate_ref, dggate_ref, dghg_ref, dggla_ref, dy1_ref, dbhg_ref, dbgla_ref,
             stat_ref):
        i = pl.program_id(0)

        @pl.when(i == 0)
        def _():
            stat_ref[...] = jnp.zeros_like(stat_ref)

        @pl.when(i < n_ctx_tiles)
        def _():
            for ref in (dohg_ref, dogla_ref, dhgate_ref, dggate_ref, dghg_ref, dggla_ref):
                ref[...] = jnp.zeros_like(ref)

        @pl.when(i >= n_ctx_tiles)
        def _():
            post1, gate1 = norm_ref[1:2, :], modx_ref[2:3, :]
            hg_on, gla_on = on_ref[0:1, 0:HD], on_ref[1:2, 0:HD]
            p_gate_hg = jnp.concatenate([p_ghg_a[...], p_ghg_b[...]], axis=1)
            p_gate_gla = jnp.concatenate([p_ggla_a[...], p_ggla_b[...]], axis=1)
            ph, pg = p_hgate[...], p_ggate[...]
            t = _mixer_tail(x_ref[...], ofw_hg[...] + obw_hg[...], ofw_gla[...] + obw_gla[...], ph, pg,
                            p_gate_hg, p_gate_gla, hg_on, gla_on, wbh_ref[...], wbg_ref[...], wout_ref[...])
            dz2 = dz2_ref[...]
            n1, r1 = _rms(y1_ref[...])
            dgate1 = _colsum(dz2 * n1 * post1)
            tt = dz2 * gate1
            dpost1 = _colsum(tt * n1)
            dy1 = _rms_bwd(tt * post1, n1, r1).astype(BF16)
            dy1_ref[...] = dy1
            dmerged = _dot_nt(dy1, wout_ref[...])
            dghg_ref[...] = (dmerged * t["b_hg"] * t["s_hg"] * (1.0 - t["s_hg"])).astype(BF16)
            dggla_ref[...] = (dmerged * t["b_gla"] * t["s_gla"] * (1.0 - t["s_gla"])).astype(BF16)
            db_hg = (dmerged * t["s_hg"]).astype(BF16)
            db_gla = (dmerged * t["s_gla"]).astype(BF16)
            dbhg_ref[...] = db_hg
            dbgla_ref[...] = db_gla
            don_acc = []
            for (db, wb, pgate, on, ns, rs, gain, gate_ref, do_ref) in (
                    (db_hg, wbh_ref, ph, t["on_hg"], t["n_hg"], t["r_hg"], hg_on, dhgate_ref, dohg_ref),
                    (db_gla, wbg_ref, pg, t["on_gla"], t["n_gla"], t["r_gla"], gla_on, dggate_ref, dogla_ref)):
                dog = _dot_nt(db, wb[...])
                gate_ref[...] = (dog * on * _dsilu(pgate)).astype(BF16)
                don = dog * _silu(pgate)
                acc = jnp.zeros((1, HD), F32)
                for h in range(NH):
                    sl = slice(h * HD, (h + 1) * HD)
                    acc = acc + _colsum(don[:, sl] * ns[h])
                    do_ref[:, sl] = _rms_bwd(don[:, sl] * gain, ns[h], rs[h]).astype(BF16)
                don_acc.append(acc)
            stat_ref[0:1, :] += dgate1
            stat_ref[1:2, :] += dpost1
            stat_ref[2:3, 0:HD] += don_acc[0]
            stat_ref[2:3, HD:2 * HD] += don_acc[1]

    lat = lambda i: (jnp.maximum(i - n_ctx_tiles, 0), 0)
    full = lambda i: (i, 0)
    fixed = lambda i: (0, 0)

    def pcol(blk):
        return pl.BlockSpec((TM, HW), lambda i: (i, blk))

    in_specs = ([pl.BlockSpec((TM, d), lat)] + [pl.BlockSpec((TM, HW), full)] * 4
                + [pcol(C_HGATE), pcol(C_GGATE), pcol(9), pcol(10), pcol(11), pcol(12)]
                + [pl.BlockSpec((TM, d), lat), pl.BlockSpec((TM, d), lat)]
                + [pl.BlockSpec((8, d), fixed)] * 3 + [VMEM_SPEC] * 3)
    f = lambda w: jax.ShapeDtypeStruct((total, w), BF16)
    out_shape = [f(HW), f(HW), f(HW), f(HW), f(d), f(d), jax.ShapeDtypeStruct((rows, d), BF16),
                 jax.ShapeDtypeStruct((rows, d), BF16), jax.ShapeDtypeStruct((rows, d), BF16),
                 jax.ShapeDtypeStruct((8, d), F32)]
    out_specs = ([pl.BlockSpec((TM, HW), full)] * 4 + [pl.BlockSpec((TM, d), full)] * 2
                 + [pl.BlockSpec((TM, d), lat)] * 3 + [pl.BlockSpec((8, d), fixed)])
    return pl.pallas_call(
        body, name=name, grid=(n_tiles,), out_shape=out_shape, in_specs=in_specs, out_specs=out_specs,
        compiler_params=_cparams(dimension_semantics=("arbitrary",)),
    )(x_lat, *o_list, p, p, p, p, p, p, dz2, y1, modx, norms, onorms, w_br_hg, w_br_gla, w_out)


def _in_projection_bwd(ctx0, x0, dz2, modc, modx, pre1, w_t, pieces, n_ctx_tiles, name):
    d = x0.shape[1]
    rows = ctx0.shape[0] + x0.shape[0]
    lat_rows = dz2.shape[0]
    width = P_WIDTH
    n_pieces = len(pieces)

    def body(*refs):
        ctx_ref, x_ref, dz2_ref, modc_ref, modx_ref, pre_ref, w_ref = refs[:7]
        (dhq_f, dhq_b, dhi_f, dhi_b, dhf_f, dhf_b, dhgate, dgq_f, dgq_b, dgk_f, dgk_b, dgv_f, dgv_b, dggate,
         dghg, dggla, dlr_f, dlr_b) = refs[7:7 + n_pieces]
        dp_ref, gx_ref, stat_ref = refs[7 + n_pieces:]
        i = pl.program_id(0)
        is_ctx = i < n_ctx_tiles
        z = jnp.where(is_ctx, ctx_ref[...], x_ref[...])
        sections = [
            (0, dhq_f[...] + dhq_b[...]), (HW, dhi_f[...] + dhi_b[...]), (2 * HW, dhf_f[...]), (3 * HW, dhf_b[...]),
            (4 * HW, dhgate[...]), (5 * HW, dgq_f[...] + dgq_b[...]), (6 * HW, dgk_f[...] + dgk_b[...]),
            (7 * HW, dgv_f[...] + dgv_b[...]), (8 * HW, dggate[...]),
            (9 * HW, dghg[:, 0:HW]), (10 * HW, dghg[:, HW:2 * HW]),
            (11 * HW, dggla[:, 0:HW]), (12 * HW, dggla[:, HW:2 * HW]), (OFF_LR, dlr_f[...] + dlr_b[...])]
        dh = jnp.zeros((TM, d), F32)
        for off, val in sections:
            w = val.shape[1]
            vb = val.astype(BF16)
            dp_ref[off:off + w, :] = vb.T
            dh = dh + jnp.dot(vb, w_ref[_w_in_row(off):_w_in_row(off) + w, :], preferred_element_type=F32)
        n, r = _rms(z)
        pre = pre_ref[...]
        scale = jnp.where(is_ctx, modc_ref[1:2, :], modx_ref[1:2, :])
        nw = n * pre
        dshift = _colsum(dh)
        dscale = _colsum(dh * nw)
        dnw = dh * (1.0 + scale)
        dpre = _colsum(dnw * n)
        gx_ref[...] = dz2_ref[...] + _rms_bwd(dnw * pre, n, r)
        zero = jnp.zeros((1, d), F32)

        @pl.when(i == 0)
        def _():
            stat_ref[...] = jnp.zeros_like(stat_ref)

        stat_ref[0:1, :] += jnp.where(is_ctx, zero, dshift)
        stat_ref[1:2, :] += jnp.where(is_ctx, zero, dscale)
        stat_ref[2:3, :] += jnp.where(is_ctx, dshift, zero)
        stat_ref[3:4, :] += jnp.where(is_ctx, dscale, zero)
        stat_ref[4:5, :] += dpre

    full = lambda i: (i, 0)
    lat = lambda i: (jnp.maximum(i - n_ctx_tiles, 0), 0)
    fixed = lambda i: (0, 0)
    piece_specs = [pl.BlockSpec((TM, a.shape[1]), full) for a in pieces]
    in_specs = [pl.BlockSpec((TM, d), lambda i: (jnp.minimum(i, n_ctx_tiles - 1), 0)), pl.BlockSpec((TM, d), lat),
                pl.BlockSpec((TM, d), lat), pl.BlockSpec((8, d), fixed),
                pl.BlockSpec((8, d), fixed), pl.BlockSpec((1, d), fixed), VMEM_SPEC] + piece_specs
    return pl.pallas_call(
        body, name=name, grid=(rows // TM,),
        out_shape=[jax.ShapeDtypeStruct((width, rows), BF16), jax.ShapeDtypeStruct((lat_rows, d), F32),
                   jax.ShapeDtypeStruct((8, d), F32)],
        in_specs=in_specs,
        out_specs=[pl.BlockSpec((width, TM), lambda i: (0, i)), pl.BlockSpec((TM, d), lat),
                   pl.BlockSpec((8, d), fixed)],
        compiler_params=_cparams(dimension_semantics=("arbitrary",)),
    )(ctx0, x0, dz2, modc, modx, pre1, w_t, *pieces)


def _transposed_lhs_matmul(x_ref, dy_ref, o_ref, xt_ref):
    @pl.when(pl.program_id(1) == 0)
    def _():
        xt_ref[...] = x_ref[...].T

    o_ref[...] = jnp.dot(xt_ref[...], dy_ref[...], preferred_element_type=F32)


def _w_in_grad(dp_t, h1, n_cols, name, after=()):
    rows, d = h1.shape
    n_main = OFF_LR // HW
    lr0 = _w_in_row(OFF_LR)

    def body(x_ref, xlr_ref, h_ref, *rest):
        o_hbm, acc_ref, sems = rest[len(after):]
        i = pl.program_id(0)
        slot = i % 2

        def main_copy(step):
            row = jnp.where(step < 9, step * HW, step * HW + 2 * RANK)
            return pltpu.make_async_copy(acc_ref.at[step % 2], o_hbm.at[pl.ds(pl.multiple_of(row, 8), HW), :],
                                         sems.at[step % 2])

        @pl.when(i > 1)
        def _():
            main_copy(i - 2).wait()

        @pl.when(i < n_main)
        def _():
            acc_ref[slot] = jnp.dot(x_ref[...], h_ref[...], preferred_element_type=F32)
            main_copy(i).start()

        @pl.when(i == n_main)
        def _():
            acc_ref[slot, 0:128, :] = jnp.dot(xlr_ref[...], h_ref[...], preferred_element_type=F32)
            lr_copy = pltpu.make_async_copy(acc_ref.at[slot, 0:2 * RANK, :], o_hbm.at[lr0:lr0 + 2 * RANK, :],
                                            sems.at[slot])
            lr_copy.start()
            main_copy(i - 1).wait()
            lr_copy.wait()

    return pl.pallas_call(
        body, name=name, grid=(n_main + 1,),
        out_shape=jax.ShapeDtypeStruct((n_cols, d), F32),
        in_specs=[pl.BlockSpec((HW, rows), lambda i: (jnp.minimum(i, n_main - 1), 0)),
                  pl.BlockSpec((128, rows), lambda i: (OFF_LR // 128, 0)),
                  pl.BlockSpec((rows, d), lambda i: (0, 0))] + [ANY_SPEC] * len(after),
        out_specs=ANY_SPEC,
        scratch_shapes=[pltpu.VMEM((2, HW, d), F32), pltpu.SemaphoreType.DMA((2,))],
        compiler_params=_cparams(dimension_semantics=("arbitrary",)),
    )(dp_t, dp_t, h1, *after)


def _weight_grad(xs, dy, name, tk=None, tn=512, k_first=0, k_tiles=None):
    rows = dy.shape[0]
    n = dy.shape[1]
    tn_ = min(tn, n)
    tk_ = xs.shape[1] if tk is None else tk
    k_tiles = xs.shape[1] // tk_ if k_tiles is None else k_tiles
    k = k_tiles * tk_

    return pl.pallas_call(
        functools.partial(_transposed_lhs_matmul), name=name, grid=(k_tiles, n // tn_),
        out_shape=jax.ShapeDtypeStruct((k, n), F32),
        in_specs=[pl.BlockSpec((rows, tk_), lambda i, j: (0, i + k_first)),
                  pl.BlockSpec((rows, tn_), lambda i, j: (0, j))],
        out_specs=pl.BlockSpec((tk_, tn_), lambda i, j: (i, j)),
        scratch_shapes=[pltpu.VMEM((tk_, rows), BF16)],
        compiler_params=_cparams(dimension_semantics=("parallel", "arbitrary")),
    )(xs, dy)


def _running_sums(xs, fws):
    c = xs[0].shape[0]
    row = lax.broadcasted_iota(jnp.int32, (c, 1), 0)
    s = 1
    while s < c:
        xs = [x + (jnp.where(row >= s, pltpu.roll(x, s, axis=0), 0.0) if fw else
                   jnp.where(row < c - s, pltpu.roll(x, c - s, axis=0), 0.0)) for x, fw in zip(xs, fws)]
        s *= 2
    return xs


def _chunks_terms(qs, ks, gs, fws):
    c = CHUNK
    n = len(qs)
    r = lax.broadcasted_iota(jnp.int32, (c, c), 0)
    s = lax.broadcasted_iota(jnp.int32, (c, c), 1)
    row = lax.broadcasted_iota(jnp.int32, (c, 1), 0)
    per_dir = {}
    for fw in set(fws):
        pos = row if fw else (c - 1 - row)
        per_dir[fw] = dict(
            causal=(s <= r) if fw else (s >= r), causal_t=(s >= r) if fw else (s <= r), pos=pos,
            in_blk=[(pos >= SUB * j) & (pos < SUB * (j + 1)) for j in range(NSUB)],
            start_row=[None] + [SUB * j - 1 if fw else c - SUB * j for j in range(1, NSUB)],
            rend=c - 1 if fw else 0)
    dirs = [per_dir[fw] for fw in fws]
    cums = _running_sums(gs, fws)
    starts = [[None] + [cum[d["start_row"][j]:d["start_row"][j] + 1, :] for j in range(1, NSUB)]
              for cum, d in zip(cums, dirs)]
    es = [[jnp.exp(cum) for cum in cums]]
    for j in range(1, NSUB):
        es.append([jnp.exp(jnp.where(d["pos"] >= SUB * j, cum - st[j], -1e30)) for cum, st, d in zip(cums, starts, dirs)])
    owns = [functools.reduce(lambda rest, j: jnp.where(d["in_blk"][j], st[j], rest), range(1, NSUB), 0.0)
            for st, d in zip(starts, dirs)]
    kscales = [jnp.exp(own - cum) for own, cum in zip(owns, cums)]
    cends = [cum[d["rend"]:d["rend"] + 1, :] for cum, d in zip(cums, dirs)]
    tails = [jnp.exp(cend - cum) for cend, cum in zip(cends, cums)]
    qcats = [jnp.concatenate([q * es[j][i] for j in range(NSUB)], axis=1).astype(BF16) for i, q in enumerate(qs)]
    kts = [k * ksc for k, ksc in zip(ks, kscales)]
    kms = [jnp.concatenate([jnp.where(d["in_blk"][j], kt, 0.0) for j in range(NSUB)], axis=1).astype(BF16)
           for kt, d in zip(kts, dirs)]
    e_by_lane = [[es[j][i] for j in range(NSUB)] for i in range(n)]
    return dict(dirs=dirs, e=e_by_lane, kscale=kscales, cend=cends, tail=tails, qcat=qcats, km=kms, kt=kts)


def _chunks_fwd(qs, ks, vs, gs, st0s, fws):
    t = _chunks_terms(qs, ks, gs, fws)
    scores = [_dot_nt(qc, km) for qc, km in zip(t["qcat"], t["km"])]
    a = [jnp.where(d["causal"], sc, 0.0) for sc, d in zip(scores, t["dirs"])]
    inter = [_dot_nt(qc[:, 0:HD], st0) for qc, st0 in zip(t["qcat"], st0s)]
    intra = [_dot(a_, v) for a_, v in zip(a, vs)]
    os_ = [x + y for x, y in zip(intra, inter)]
    upd = [_dot_tn(v, k * tl) for v, k, tl in zip(vs, ks, t["tail"])]
    st1s = [st0 * jnp.exp(ce) + u for st0, ce, u in zip(st0s, t["cend"], upd)]
    return os_, st1s


def _chunks_bwd(qs, ks, vs, gs, st0s, dos, dst1s, fws):
    n = len(qs)
    t = _chunks_terms(qs, ks, gs, fws)
    qcat, km, e, dirs = t["qcat"], t["km"], t["e"], t["dirs"]
    a_t = [jnp.where(d["causal_t"], _dot_nt(km_, qc), 0.0) for km_, qc, d in zip(km, qcat, dirs)]
    ktail = [k * tl for k, tl in zip(ks, t["tail"])]
    dv_a = [_dot(at, do) for at, do in zip(a_t, dos)]
    dv_b = [_dot_nt(kt, ds) for kt, ds in zip(ktail, dst1s)]
    dv = [x + y for x, y in zip(dv_a, dv_b)]
    da = [jnp.where(d["causal"], _dot_nt(do, v), 0.0) for do, v, d in zip(dos, vs, dirs)]
    da_t = [jnp.where(d["causal_t"], _dot_nt(v, do), 0.0) for do, v, d in zip(dos, vs, dirs)]
    dqcat = [_dot(da_, km_) for da_, km_ in zip(da, km)]
    dq_inter = [e[i][0] * _dot(dos[i], st0s[i]) for i in range(n)]
    dkm = [_dot(dat, qc) for dat, qc in zip(da_t, qcat)]
    dk_inter = [_dot(v, ds) * tl for v, ds, tl in zip(vs, dst1s, t["tail"])]
    dq = [dq_inter[i] + sum(e[i][j] * dqcat[i][:, j * HD:(j + 1) * HD] for j in range(NSUB)) for i in range(n)]
    dkt = [sum(jnp.where(dirs[i]["in_blk"][j], dkm[i][:, j * HD:(j + 1) * HD], 0.0) for j in range(NSUB))
           for i in range(n)]
    dk = [dkt[i] * t["kscale"][i] + dk_inter[i] for i in range(n)]
    dcum = [qs[i] * dq_inter[i] - ks[i] * dk_inter[i] - t["kt"][i].astype(BF16).astype(F32) * dkt[i]
            + sum(qcat[i][:, j * HD:(j + 1) * HD].astype(F32) * dqcat[i][:, j * HD:(j + 1) * HD] for j in range(NSUB))
            for i in range(n)]
    ecend = [jnp.exp(ce) for ce in t["cend"]]
    end = [ecend[i] * _colsum(st0s[i] * dst1s[i]) + _colsum(ks[i] * dk_inter[i]) for i in range(n)]
    sums = _running_sums(dcum, [not fw for fw in fws])
    dg = [sm + en for sm, en in zip(sums, end)]
    upd = [_dot_tn(dos[i], qs[i] * e[i][0]) for i in range(n)]
    dst0 = [dst1s[i] * ecend[i] + upd[i] for i in range(n)]
    return dq, dk, dv, dg, dst0


def _chunk_index(step, n_ctx_chunks, n_chunks, fw):
    if fw:
        return step
    return jnp.where(step < n_ctx_chunks, n_ctx_chunks - 1 - step, n_chunks - 1 + n_ctx_chunks - step)


def _hg_inputs(hq, hf, lbv, d_idx, sl):
    lb = _sigmoid(lbv[d_idx:d_idx + 1, sl] - lbv[2 + d_idx:3 + d_idx, sl])
    sg = _sigmoid(hf)
    f = lb + (1.0 - lb) * sg
    return _silu(hq), 1.0 - f, jnp.log(f), f, sg, lb


def _scan_fwd_both(p, branch_sides, n_ctx_chunks, name):
    rows = p.shape[0]
    n_chunks = rows // CHUNK
    n_ins = [4 if branch == "hg" else 6 for branch, _ in branch_sides]
    n_in_all = 2 * sum(n_ins)
    n_br = len(branch_sides)

    def body(*refs):
        ins, outs, state = refs[:n_in_all], refs[n_in_all:n_in_all + 4 * n_br], refs[-1]

        @pl.when(pl.program_id(0) == 0)
        def _():
            state[...] = jnp.zeros_like(state)

        lanes, where = [], []
        pos = 0
        for bi, (branch, _) in enumerate(branch_sides):
            hg = branch == "hg"
            n_in = n_ins[bi]
            for di, fw in enumerate((True, False)):
                r = ins[pos:pos + n_in]
                pos += n_in
                o_ref, st_ref = outs[4 * bi + 2 * di], outs[4 * bi + 2 * di + 1]
                if hg:
                    a_ref, b_ref, c_ref, lb_ref = r
                else:
                    a_ref, b_ref, c_ref, lr_ref, wgk_ref, bgk_ref = r
                    logits = _dot(lr_ref[...], wgk_ref[...]) + bgk_ref[...]
                    g_all = _log_sigmoid(logits) * (1.0 / GATE_NORM)
                for h in range(NH):
                    sl = slice(h * HD, (h + 1) * HD)
                    if hg:
                        q, k, g, _, _, _ = _hg_inputs(a_ref[:, sl], c_ref[:, sl], lb_ref[...], di, sl)
                        v = b_ref[:, sl]
                    else:
                        q, k, v, g = a_ref[:, sl] * (HD ** -0.5), b_ref[:, sl], c_ref[:, sl], g_all[:, sl]
                    lanes.append((q, k, v, g, state[2 * bi + di, h], fw))
                    where.append((2 * bi + di, h, sl, o_ref, st_ref))
        qs, ks, vs, gs, st0s, fws = (list(col) for col in zip(*lanes))
        os_, st1s = _chunks_fwd(qs, ks, vs, gs, st0s, fws)
        for (si, h, sl, o_ref, st_ref), st0, o, st1 in zip(where, st0s, os_, st1s):
            st_ref[0, h] = st0
            o_ref[:, sl] = o
            state[si, h] = st1

    fixed = lambda j: (0, 0)
    in_specs, args, out_specs = [], [], []
    for branch, side in branch_sides:
        for di, fw in enumerate((True, False)):
            chunk = functools.partial(_chunk_index, n_ctx_chunks=n_ctx_chunks, n_chunks=n_chunks, fw=fw)

            def cmap(blk, width=HW, chunk=chunk):
                return pl.BlockSpec((CHUNK, width), lambda j: (chunk(j), blk))

            if branch == "hg":
                in_specs += [cmap(C_HQ), cmap(C_HI), cmap(C_HF_FW + di), pl.BlockSpec((4, HW), fixed)]
                args += [p, p, p, side]
            else:
                in_specs += [cmap(C_GQ), cmap(C_GK), cmap(C_GV), cmap(OFF_LR // 128, 128),
                             pl.BlockSpec((128, HW), fixed), pl.BlockSpec((1, HW), fixed)]
                args += [p, p, p, p, side[di][0], side[di][1]]
            out_specs += [cmap(0), pl.BlockSpec((1, NH, HD, HD), lambda j, chunk=chunk: (chunk(j), 0, 0, 0))]
    return pl.pallas_call(
        body, name=name, grid=(n_chunks,),
        out_shape=[jax.ShapeDtypeStruct((rows, HW), F32),
                   jax.ShapeDtypeStruct((n_chunks, NH, HD, HD), F32)] * (2 * n_br),
        in_specs=in_specs, out_specs=out_specs,
        scratch_shapes=[pltpu.VMEM((2 * n_br, NH, HD, HD), F32)],
        compiler_params=_cparams(dimension_semantics=("arbitrary",)),
    )(*args)


def _scan_bwd_both(p, branch_items, n_ctx_chunks, name):
    rows = p.shape[0]
    n_chunks = rows // CHUNK
    n_ins = [6 if item[0] == "hg" else 8 for item in branch_items]
    n_outs = [4 if item[0] == "hg" else 6 for item in branch_items]
    n_in_all, n_out_all = 2 * sum(n_ins), 2 * sum(n_outs)

    def body(*refs):
        ins, outs, dstate = refs[:n_in_all], refs[n_in_all:n_in_all + n_out_all], refs[-1]
        first = pl.program_id(0) == 0

        @pl.when(first)
        def _():
            dstate[...] = jnp.zeros_like(dstate)

        lanes, where, extra, ctx = [], [], [], []
        ipos = opos = 0
        for bi, item in enumerate(branch_items):
            hg = item[0] == "hg"
            for di, fw in enumerate((True, False)):
                r, w = ins[ipos:ipos + n_ins[bi]], outs[opos:opos + n_outs[bi]]
                ipos += n_ins[bi]
                opos += n_outs[bi]
                if hg:
                    a_ref, b_ref, c_ref, lb_ref, st_ref, do_ref = r
                    acc_refs = (w[3],)
                else:
                    a_ref, b_ref, c_ref, lr_ref, wgk_ref, bgk_ref, st_ref, do_ref = r
                    acc_refs = (w[4], w[5])
                    lr = lr_ref[...]
                    logits = _dot(lr, wgk_ref[...]) + bgk_ref[...]
                    g_all = _log_sigmoid(logits) * (1.0 / GATE_NORM)

                @pl.when(first)
                def _(acc_refs=acc_refs):
                    for ref in acc_refs:
                        ref[...] = jnp.zeros_like(ref)

                for h in range(NH):
                    sl = slice(h * HD, (h + 1) * HD)
                    if hg:
                        hq, hf = a_ref[:, sl], c_ref[:, sl]
                        q, k, g, f, sg, lb = _hg_inputs(hq, hf, lb_ref[...], di, sl)
                        v = b_ref[:, sl]
                        extra.append((hq, f, sg, lb))
                    else:
                        q, k, v, g = a_ref[:, sl] * (HD ** -0.5), b_ref[:, sl], c_ref[:, sl], g_all[:, sl]
                        extra.append(None)
                    lanes.append((q, k, v, g, st_ref[0, h], do_ref[:, sl], dstate[2 * bi + di, h], fw))
                    where.append((2 * bi + di, h, sl))
                ctx.append((hg, w, None if hg else (lr, logits, wgk_ref)))

        dqs, dks, dvs, dgs, dst0s = [], [], [], [], []
        for lo in range(0, len(lanes), BWD_LANES):
            cols = [list(col) for col in zip(*lanes[lo:lo + BWD_LANES])]
            for acc, part in zip((dqs, dks, dvs, dgs, dst0s), _chunks_bwd(*cols)):
                acc.extend(part)
        dg_parts = [[] for _ in ctx]
        for (si, h, sl), ex, dq, dk, dv, dg, dst0 in zip(where, extra, dqs, dks, dvs, dgs, dst0s):
            dstate[si, h] = dst0
            hg, w, _ = ctx[si]
            if hg:
                hq, f, sg, lb = ex
                da_ref, db_ref, dc_ref, dlb_ref = w
                da_ref[:, sl] = (dq * _dsilu(hq)).astype(BF16)
                db_ref[:, sl] = dv.astype(BF16)
                df = dg / f - dk
                dc_ref[:, sl] = (df * (1.0 - lb) * sg * (1.0 - sg)).astype(BF16)
                dlb_ref[0:1, sl] += _colsum(df * (1.0 - sg))
            else:
                da_ref, db_ref, dc_ref = w[:3]
                da_ref[:, sl] = (dq * (HD ** -0.5)).astype(BF16)
                db_ref[:, sl] = dk.astype(BF16)
                dc_ref[:, sl] = dv.astype(BF16)
                dg_parts[si].append(dg)
        for si, (hg, w, more) in enumerate(ctx):
            if not hg:
                dlr_ref, dwgk_ref, dbias_ref = w[3:]
                lr, logits, wgk_ref = more
                dlogits = jnp.concatenate(dg_parts[si], axis=1) * (1.0 / GATE_NORM) * (1.0 - _sigmoid(logits))
                dlr_ref[...] = _dot_nt(dlogits, wgk_ref[...]).astype(BF16)
                dwgk_ref[...] += _dot_tn(lr, dlogits)
                dbias_ref[0:1, :] += _colsum(dlogits)

    fixed = lambda j: (0, 0)
    big = jax.ShapeDtypeStruct((rows, HW), BF16)
    in_specs, args, out_shape, out_specs = [], [], [], []
    for branch, side, states, d_o in branch_items:
        for di, fw in enumerate((True, False)):
            def chunk_of(j, fw=fw):
                return _chunk_index(n_chunks - 1 - j, n_ctx_chunks, n_chunks, fw)

            def cmap(blk, width=HW, chunk_of=chunk_of):
                return pl.BlockSpec((CHUNK, width), lambda j: (chunk_of(j), blk))

            st_spec = pl.BlockSpec((1, NH, HD, HD), lambda j, chunk_of=chunk_of: (chunk_of(j), 0, 0, 0))
            if branch == "hg":
                in_specs += [cmap(C_HQ), cmap(C_HI), cmap(C_HF_FW + di), pl.BlockSpec((4, HW), fixed), st_spec,
                             cmap(0)]
                args += [p, p, p, side, states[di], d_o]
                out_shape += [big, big, big, jax.ShapeDtypeStruct((8, HW), F32)]
                out_specs += [cmap(0), cmap(0), cmap(0), pl.BlockSpec((8, HW), fixed)]
            else:
                in_specs += [cmap(C_GQ), cmap(C_GK), cmap(C_GV), cmap(OFF_LR // 128, 128),
                             pl.BlockSpec((128, HW), fixed), pl.BlockSpec((1, HW), fixed), st_spec, cmap(0)]
                args += [p, p, p, p, side[di][0], side[di][1], states[di], d_o]
                out_shape += [big, big, big, jax.ShapeDtypeStruct((rows, 128), BF16),
                              jax.ShapeDtypeStruct((128, HW), F32), jax.ShapeDtypeStruct((8, HW), F32)]
                out_specs += [cmap(0), cmap(0), cmap(0), cmap(0, 128), pl.BlockSpec((128, HW), fixed),
                              pl.BlockSpec((8, HW), fixed)]
    return pl.pallas_call(
        body, name=name, grid=(n_chunks,), out_shape=out_shape, in_specs=in_specs, out_specs=out_specs,
        scratch_shapes=[pltpu.VMEM((2 * len(branch_items), NH, HD, HD), F32)],
        compiler_params=_cparams(dimension_semantics=("arbitrary",)),
    )(*args)


SMALL_ROWS = 56
ROWS_MOD_X = (0, 1, 8, 16, 17, 18)
ROWS_MOD_C = (2, 3)
ROW_PRE1, ROW_POST1, ROW_ONORM, ROW_PRE2, ROW_POST2, ROW_LB, ROW_BGK, ROW_WGK = 4, 9, 10, 19, 20, 24, 32, 40
ROW_LOSS = 21


def _reduce_small(gathered, lb_full, name):
    _, _, d = gathered.shape

    def body(g_ref, lb_ref, sum_ref, dmod_ref, dbmod_ref, dlb_ref):
        total = g_ref[0]
        for b in range(1, N_DEV):
            total = total + g_ref[b]
        sum_ref[...] = total
        dmod_ref[...] = jnp.zeros_like(dmod_ref)
        for m in range(N_MOD):
            col = slice(m * d, (m + 1) * d)
            acc = jnp.zeros((1, d), F32)
            for b in range(N_DEV):
                row = g_ref[b, ROWS_MOD_X[m]:ROWS_MOD_X[m] + 1, :]
                dmod_ref[b:b + 1, col] = row
                acc = acc + row
            if m < 2:
                ctx_row = total[ROWS_MOD_C[m]:ROWS_MOD_C[m] + 1, :]
                dmod_ref[8:9, col] = ctx_row
                acc = acc + ctx_row
            dbmod_ref[:, col] = acc
        lbv = lb_ref[...]
        for dd in range(2):
            lb = _sigmoid(lbv[dd:dd + 1, :] - lbv[2 + dd:3 + dd, :])
            gl = total[ROW_LB:ROW_LB + 1, dd * HW:(dd + 1) * HW] * lb * (1.0 - lb)
            dlb_ref[dd:dd + 1, :] = gl
            dlb_ref[2 + dd:3 + dd, :] = -gl

    return pl.pallas_call(
        body, name=name,
        out_shape=[jax.ShapeDtypeStruct((SMALL_ROWS, d), F32), jax.ShapeDtypeStruct((16, N_MOD * d), F32),
                   jax.ShapeDtypeStruct((1, N_MOD * d), F32), jax.ShapeDtypeStruct((4, HW), F32)],
        in_specs=[VMEM_SPEC] * 2, out_specs=[VMEM_SPEC] * 4, compiler_params=_cparams(),
    )(gathered, lb_full)


def _c_ctx_grad(gathered, c_ctx_row, name):
    def body(g_ref, c_ref, o_ref):
        acc = g_ref[0, 0:1, :]
        for chip in range(1, N_CHIP):
            acc = acc + g_ref[2 * chip, 0:1, :]
        o_ref[...] = acc * _dsilu(c_ref[...])

    return pl.pallas_call(
        body, name=name, out_shape=jax.ShapeDtypeStruct(c_ctx_row.shape, F32),
        in_specs=[VMEM_SPEC] * 2, out_specs=VMEM_SPEC, compiler_params=_cparams(),
    )(gathered, c_ctx_row)


def _blocked(full, n_blocks):
    k, n = full.shape
    return full.reshape(k, n_blocks, n // n_blocks).transpose(1, 0, 2)


def _unblocked(blocks):
    nb, k, n = blocks.shape
    return blocks.transpose(1, 0, 2).reshape(k, nb * n)


def _sample_front(x0, ctx0, modc, modx, norm_pre1, lb_full, gla_side, w_in_r):
    ctx_len = ctx0.shape[0]
    n_ctx_tiles = ctx_len // TM
    n_ctx_chunks = ctx_len // CHUNK
    h1, p = _in_projection(ctx0, x0, modc, modx, norm_pre1, w_in_r, n_ctx_tiles, "in_projection")
    (o_hg_fw, st_hg_fw, o_hg_bw, st_hg_bw, o_gla_fw, st_gla_fw, o_gla_bw, st_gla_bw) = _scan_fwd_both(
        p, [("hg", lb_full), ("gla", gla_side)], n_ctx_chunks, "scan_fwd")
    return dict(h1=h1, p=p, o_list=[o_hg_fw, o_hg_bw, o_gla_fw, o_gla_bw],
                states=[st_hg_fw, st_hg_bw, st_gla_fw, st_gla_bw])


def _sample_back(reduce, front, x0, ctx0, target0, modc, modx, norm_pre1, norms, onorms, lb_full, gla_side, w_in_r,
                 wbh, wbg, wout, ffn_weights):
    seq, d = x0.shape
    ctx_len = ctx0.shape[0]
    n_ctx_tiles = ctx_len // TM
    n_tiles = (ctx_len + seq) // TM
    n_ctx_chunks = ctx_len // CHUNK
    h1, p, o_list = front["h1"], front["p"], front["o_list"]
    st_hg_fw, st_hg_bw, st_gla_fw, st_gla_bw = front["states"]
    z2, y1, merged, og_hg, og_gla = _mixer_tail_fwd(x0, p, o_list, modx, norms, onorms, wbh, wbg, wout, n_ctx_tiles,
                                                    "mixer_tail")
    wg, wu, wd = ffn_weights([z2])
    loss_part, dz2, h2, a_act, du, dv, dy2, stat_ffn = _ffn_fwd_bwd(z2, modx, norms, wg, wu, wd, target0, "ffn")
    dff = wg.shape[0]
    tok = reduce("ffn", [_weight_grad(du, h2, "grad_w_ff_gate", tk=dff // 2, tn=d),
                         _weight_grad(dv, h2, "grad_w_ff_up", tk=dff // 2, tn=d),
                         _weight_grad(a_act, dy2, "grad_w_ff_down", tk=dff // 2)])

    (d_ohg, d_ogla, d_hgate, d_ggate, d_ghg, d_ggla, dy1, db_hg, db_gla, stat_mix) = _mixer_tail_bwd(
        x0, p, o_list, dz2, y1, modx + tok, norms, onorms, wbh, wbg, wout, n_ctx_tiles, n_tiles, "mixer_tail_bwd")
    tok = reduce("mix", [_weight_grad(og_hg, db_hg, "grad_w_br_hg"), _weight_grad(og_gla, db_gla, "grad_w_br_gla"),
                         _weight_grad(merged, dy1, "grad_w_out")])
    tok = tok + reduce("push_ffn", [dy1])
    gla_b = [(wgk, bias + tok) for wgk, bias in gla_side]
    (dgq_f, dgk_f, dgv_f, dlr_f, dwgk_f, dbgk_f, dgq_b, dgk_b, dgv_b, dlr_b, dwgk_b, dbgk_b,
     dhq_f, dhi_f, dhf_f, dlb_f, dhq_b, dhi_b, dhf_b, dlb_b) = _scan_bwd_both(
        p, [("gla", gla_b, (st_gla_fw, st_gla_bw), d_ogla), ("hg", lb_full, (st_hg_fw, st_hg_bw), d_ohg)],
        n_ctx_chunks, "scan_bwd")
    tok = reduce("push_mix", [dbgk_f])
    pieces = [dhq_f, dhq_b, dhi_f, dhi_b, dhf_f, dhf_b, d_hgate, dgq_f, dgq_b, dgk_f, dgk_b, dgv_f, dgv_b, d_ggate,
              d_ghg, d_ggla, dlr_f, dlr_b]
    dp, grad_x, stat_in = _in_projection_bwd(ctx0, x0, dz2, modc, modx, norm_pre1 + tok, w_in_r, pieces, n_ctx_tiles,
                                             "in_projection_bwd")

    started = reduce("small_start", dict(stat_in=stat_in, stat_mix=stat_mix, stat_ffn=stat_ffn, dlb=(dlb_f, dlb_b),
                                         dwgk=(dwgk_f, dwgk_b), dbgk=(dbgk_f, dbgk_b)))
    reduce("in", [_w_in_grad(dp, h1, w_in_r.shape[0], "grad_w_in", after=[started])])
    reduce("small", None)
    reduce("push_in", [])
    return dict(loss_part=loss_part, grad_x=grad_x)


def kernel(x, c, ctx, c_ctx, w_mod, b_mod, norm_pre1, norm_post1, norm_pre2, norm_post2, w_in, hg_lb, hg_onorm, gla_w_gk, gla_b_gk, gla_onorm, w_br_hg, w_br_gla, w_out, w_ff_gate, w_ff_up, w_ff_down, loss_target, m_c_ctx, m_w_mod, m_b_mod, m_norm_pre1, m_norm_post1, m_norm_pre2, m_norm_post2, m_w_in, m_hg_lb, m_hg_onorm, m_gla_w_gk, m_gla_b_gk, m_gla_onorm, m_w_br_hg, m_w_br_gla, m_w_out, m_w_ff_gate, m_w_ff_up, m_w_ff_down, v_c_ctx, v_w_mod, v_b_mod, v_norm_pre1, v_norm_post1, v_norm_pre2, v_norm_post2, v_w_in, v_hg_lb, v_hg_onorm, v_gla_w_gk, v_gla_b_gk, v_gla_onorm, v_w_br_hg, v_w_br_gla, v_w_out, v_w_ff_gate, v_w_ff_up, v_w_ff_down):
    seq, d = x.shape[1], x.shape[2]
    ctx_len = ctx.shape[1]
    assert seq % TM == 0 and ctx_len % TM == 0 and d == 2 * HW
    ax, ay, ac = lax.axis_index("x"), lax.axis_index("y"), lax.axis_index("c")
    chip = 2 * ax + ay
    dev = 2 * chip + ac
    c_arr = jnp.reshape(ac, (1,)).astype(jnp.int32)
    chip_arr = jnp.reshape(chip, (1,)).astype(jnp.int32)
    transposed = ("w_in", "w_ff_gate", "w_ff_up")
    view = lambda a, nm: a[0].T if nm in transposed else a[0]

    sems_in, lands_in, token_in0 = _blocks_start([_cast_into_blocks(chip_arr, view(w_in, "w_in"), "cast_w_in")],
                                                 "gather_w_in_start")

    nc = d // 128
    pad8 = lambda a: jnp.pad(a, ((0, -a.shape[0] % 8), (0, 0)))
    small1 = jnp.concatenate([c.reshape(nc, 128) + token_in0[0, 0], pad8(hg_lb.reshape(4, 128)),
                              gla_w_gk.reshape(2 * RANK, 128), pad8(gla_b_gk.reshape(2, 128))], axis=0)
    blocks = [_cast_into_blocks(chip_arr, view(w_, nm), "cast_" + nm) for w_, nm in (
        (w_br_hg, "w_br_hg"), (w_br_gla, "w_br_gla"), (w_out, "w_out"), (w_ff_gate, "w_ff_gate"),
        (w_ff_up, "w_ff_up"), (w_ff_down, "w_ff_down"))]
    got1 = _allgather8(small1, "gather_small_params", after=blocks)
    c_all = got1[:, :nc, :].reshape(N_DEV, d)
    per_chip = got1[0::2]
    lb_full = per_chip[:, nc:nc + 4, :].transpose(1, 0, 2).reshape(4, HW)
    wgk_full = per_chip[:, nc + 8:nc + 8 + 2 * RANK, :].transpose(1, 0, 2).reshape(2, RANK, HW)
    bgk_full = per_chip[:, nc + 8 + 2 * RANK:nc + 10 + 2 * RANK, :].transpose(1, 0, 2).reshape(2, HW)
    wgk_pad = [jnp.zeros((128, HW), F32).at[dd * RANK:(dd + 1) * RANK].set(wgk_full[dd]) for dd in range(2)]
    bgk = [bgk_full[dd:dd + 1] for dd in range(2)]

    n_mod_cols = w_mod.shape[2]
    cond = jnp.concatenate([c_all, pad8(c_ctx.reshape(1, d))], axis=0)
    b_cols = lax.dynamic_slice(b_mod, (0, chip * n_mod_cols), (1, n_mod_cols))
    lands_in = _blocks_wait(sems_in, lands_in, [got1], "gather_w_in_wait")
    fwd_sems, lands_in, fwd_token = _forward_start(lands_in, "gather_w_in_forward_start")
    mod_part = _mod_forward(cond + fwd_token[0, 0], w_mod[0], b_cols, "mod_forward")
    mod_got = _allgather8(mod_part, "gather_mod")
    mod_all = mod_got[0::2].transpose(1, 0, 2).reshape(16, N_CHIP * n_mod_cols)
    modx = pad8(lax.dynamic_slice(mod_all, (dev, 0), (1, N_MOD * d)).reshape(N_MOD, d))
    modc = pad8(mod_all[8].reshape(N_MOD, d))

    gathered_in = _forward_wait(fwd_sems, lands_in, [mod_got], "gather_w_in_forward_wait")
    sems, lands, token = _blocks_start(blocks, "gather_rest_start", after=[gathered_in[0]])
    w_in_r = gathered_in[0].reshape(-1, d)

    norms = jnp.concatenate([norm_pre1, norm_post1, norm_pre2, norm_post2, jnp.zeros((4, d), F32)], axis=0)
    onorms = jnp.zeros((8, d), F32).at[0, :HD].set(hg_onorm[0]).at[1, :HD].set(gla_onorm[0])
    gla_side = [(wgk_pad[dd], bgk[dd]) for dd in range(2)]
    modx = modx + token[0, 0]
    front = _sample_front(x[0], ctx[0], modc, modx, norm_pre1, lb_full, gla_side, w_in_r)
    lands = _blocks_wait(sems, lands, front["o_list"], "gather_rest_wait")
    gathered = _blocks_finish(lands[:3], "gather_mix_finish")
    wbh, wbg = _unblocked(gathered[0]), _unblocked(gathered[1])
    wout = gathered[2].reshape(d, d)
    ffn_sems, ffn_lands, ffn_token = _forward_start(lands[3:], "gather_ffn_forward_start")
    onorms = onorms + ffn_token[0, 0]

    def ffn_weights(after):
        got = _forward_wait(ffn_sems, ffn_lands, after, "gather_ffn_forward_wait")
        return tuple(g.reshape(-1, d) for g in got)

    dff = w_ff_down.shape[1] * N_CHIP
    groups = {"ffn": ["w_ff_gate", "w_ff_up", "w_ff_down"], "mix": ["w_br_hg", "w_br_gla", "w_out"], "in": ["w_in"]}
    row_sharded = {"w_out": d // N_CHIP, "w_ff_down": dff // N_CHIP, "w_ff_gate": dff // N_CHIP,
                   "w_ff_up": dff // N_CHIP, "w_in": w_in.shape[2]}
    in_flight, to_sibling, small = {}, {}, {}

    def reduce_small_start(stats):
        small2 = jnp.concatenate([
            stats["stat_in"], stats["stat_mix"], stats["stat_ffn"],
            jnp.concatenate(stats["dlb"], axis=1), jnp.concatenate(stats["dbgk"], axis=1),
            jnp.concatenate([stats["dwgk"][0][0:RANK], stats["dwgk"][1][RANK:2 * RANK]], axis=1)], axis=0)
        assert small2.shape[0] == SMALL_ROWS
        sems_, land_, token_ = _allgather8_start(small2, "gather_small_grads_start")
        small.update(gathering=(sems_, land_))
        return token_

    def reduce_small():
        got2 = small["gathered"]
        total, dmod_all, g_b_mod, g_lb_full = _reduce_small(got2, lb_full + small["in_started"], "reduce_small")
        dmod_cols = lax.dynamic_slice(dmod_all, (0, chip * n_mod_cols), (16, n_mod_cols))
        g_w_mod, cctx_part = _mod_backward(cond, w_mod[0], dmod_cols, "mod_backward")
        got3 = _allgather8(cctx_part, "gather_c_ctx_grad")
        g_c_ctx = _c_ctx_grad(got3, c_ctx.reshape(1, d), "c_ctx_grad")
        small.update(total=total, g_b_mod=g_b_mod, g_lb_full=g_lb_full, g_w_mod=g_w_mod, g_c_ctx=g_c_ctx)

    def reduce(group, grads):
        if group == "small_start":
            return reduce_small_start(grads)
        if group == "small":
            return reduce_small()
        if group.startswith("push_"):
            return push(group[5:], grads)
        nms = groups[group]
        after = []
        if group == "in":
            small.update(gathered=_allgather8_wait(*small["gathering"], grads, "gather_small_grads_wait"))
            after = [small["gathered"]]
        full = [g.reshape(N_CHIP, row_sharded[nm], d) if nm in row_sharded else _blocked(g, N_CHIP)
                for g, nm in zip(grads, nms)]
        sems_, full, lands_, token_ = _send_half_start(full, "grads_to_sibling_start_" + group, after)
        to_sibling[group] = (sems_, full, lands_)
        small[group + "_started"] = token_[0, 0]
        return token_[0, 0]

    def push(group, after):
        nms = groups[group]
        sems_, full, lands_ = to_sibling[group]
        if group == "in":
            after = list(after) + [small["g_c_ctx"], small["total"]]
        full, from_sibling = _send_half_wait(sems_, full, lands_, after, "grads_to_sibling_wait_" + group)
        pairs = [_pair_sum(c_arr, f, r_, "pair_sum_" + nm) for f, r_, nm in zip(full, from_sibling, nms)]
        after = [small["g_c_ctx"], small["total"]] if group == "in" else []
        sems_, pairs, lands_, token_ = _scatter_start(pairs, "grads_to_owner_start_" + group, after)
        in_flight[group] = (sems_, pairs, lands_, token_)
        return token_[0, 0]

    r = _sample_back(reduce, front, x[0], ctx[0], loss_target[0], modc, modx, norm_pre1, norms, onorms, lb_full,
                     gla_side, w_in_r, wbh, wbg, wout, ffn_weights)
    grad_x = r["grad_x"]

    weights = dict(w_in=(w_in, m_w_in, v_w_in), w_br_hg=(w_br_hg, m_w_br_hg, v_w_br_hg),
                   w_br_gla=(w_br_gla, m_w_br_gla, v_w_br_gla), w_out=(w_out, m_w_out, v_w_out),
                   w_ff_gate=(w_ff_gate, m_w_ff_gate, v_w_ff_gate), w_ff_up=(w_ff_up, m_w_ff_up, v_w_ff_up),
                   w_ff_down=(w_ff_down, m_w_ff_down, v_w_ff_down))
    names = ["w_in", "w_br_hg", "w_br_gla", "w_out", "w_ff_gate", "w_ff_up", "w_ff_down"]
    big, swapping = {}, {}

    def sum_and_swap(group, after):
        sems_, pairs, lands_, _ = in_flight[group]
        pairs, lands_ = _scatter_wait(sems_, pairs, lands_, after, "grads_to_owner_wait_" + group)
        own_half = [_sum_owner(chip_arr, pr, g, "chip_sum_" + nm) for pr, g, nm in zip(pairs, lands_, groups[group])]
        swapping[group] = _swap_start(own_half, "halves_to_sibling_start_" + group)
        return own_half[-1]

    def update(group, after):
        sems_, own_half, lands_ = swapping[group]
        own_half, other_half = _swap_wait(sems_, own_half, lands_, after, "halves_to_sibling_wait_" + group)
        done = []
        for nm, own, oth in zip(groups[group], own_half, other_half):
            w_, m_, v_ = (view(a, nm) for a in weights[nm])
            res = _adamw_halves(c_arr, own, oth, w_, m_, v_, "adamw_" + nm)
            big[nm] = [r_.T[None] if nm in transposed else r_[None] for r_ in res]
            done.append(res[1])
        return done

    token_in = in_flight["in"][3]
    summed_ffn = sum_and_swap("ffn", [token_in])
    summed_mix = sum_and_swap("mix", [summed_ffn])

    total, g_b_mod, g_lb_full, g_w_mod, g_c_ctx = (small[k] for k in ("total", "g_b_mod", "g_lb_full", "g_w_mod",
                                                                      "g_c_ctx"))
    g_pre1, g_post1, g_pre2, g_post2 = (total[r_:r_ + 1] for r_ in (ROW_PRE1, ROW_POST1, ROW_PRE2, ROW_POST2))
    g_hg_on, g_gla_on = total[ROW_ONORM:ROW_ONORM + 1, 0:HD], total[ROW_ONORM:ROW_ONORM + 1, HD:2 * HD]
    n_lb = hg_lb.shape[2]
    g_hg_lb = lax.dynamic_slice(g_lb_full, (0, chip * n_lb), (4, n_lb))
    g_bgk = lax.dynamic_slice(total[ROW_BGK:ROW_BGK + 1].reshape(2, HW), (0, chip * n_lb), (2, n_lb))
    g_wgk_full = total[ROW_WGK:ROW_WGK + RANK].reshape(RANK, 2, HW).transpose(1, 0, 2).reshape(2 * RANK, HW)
    g_wgk = lax.dynamic_slice(g_wgk_full, (0, chip * n_lb), (2 * RANK, n_lb))

    small_items = [
        (g_c_ctx, c_ctx.reshape(1, d), m_c_ctx.reshape(1, d), v_c_ctx.reshape(1, d)),
        (g_b_mod, b_mod, m_b_mod, v_b_mod),
        (g_pre1, norm_pre1, m_norm_pre1, v_norm_pre1),
        (g_post1, norm_post1, m_norm_post1, v_norm_post1),
        (g_pre2, norm_pre2, m_norm_pre2, v_norm_pre2),
        (g_post2, norm_post2, m_norm_post2, v_norm_post2),
        (g_hg_lb, hg_lb.reshape(4, n_lb), m_hg_lb.reshape(4, n_lb), v_hg_lb.reshape(4, n_lb)),
        (g_hg_on, hg_onorm, m_hg_onorm, v_hg_onorm),
        (g_wgk, gla_w_gk.reshape(2 * RANK, n_lb), m_gla_w_gk.reshape(2 * RANK, n_lb), v_gla_w_gk.reshape(2 * RANK, n_lb)),
        (g_bgk, gla_b_gk.reshape(2, n_lb), m_gla_b_gk.reshape(2, n_lb), v_gla_b_gk.reshape(2, n_lb)),
        (g_gla_on, gla_onorm, m_gla_onorm, v_gla_onorm),
    ]
    small_res = _adamw_whole(small_items, "adamw_small")
    mod_res = _adamw_tiled(g_w_mod, w_mod[0], m_w_mod[0], v_w_mod[0], "adamw_w_mod")
    done_ffn = update("ffn", [summed_mix, mod_res[0], small_res[0][0]])
    done_mix = update("mix", done_ffn)
    update("in", [sum_and_swap("in", done_mix)])

    loss = total[ROW_LOSS, 0]

    shapes = dict(c_ctx=c_ctx.shape, b_mod=b_mod.shape, norm_pre1=norm_pre1.shape, norm_post1=norm_post1.shape,
                  norm_pre2=norm_pre2.shape, norm_post2=norm_post2.shape, hg_lb=hg_lb.shape, hg_onorm=hg_onorm.shape,
                  gla_w_gk=gla_w_gk.shape, gla_b_gk=gla_b_gk.shape, gla_onorm=gla_onorm.shape)
    small_names = ["c_ctx", "b_mod", "norm_pre1", "norm_post1", "norm_pre2", "norm_post2", "hg_lb", "hg_onorm",
                   "gla_w_gk", "gla_b_gk", "gla_onorm"]
    grads, deltas, new_m, new_v = {}, {}, {}, {}
    for nm, item, res in zip(small_names, small_items, small_res):
        grads[nm] = item[0].reshape(shapes[nm])
        deltas[nm], new_m[nm], new_v[nm] = (r_.reshape(shapes[nm]) for r_ in res)
    grads["w_mod"] = g_w_mod[None]
    deltas["w_mod"], new_m["w_mod"], new_v["w_mod"] = (r_[None] for r_ in mod_res)
    for nm in names:
        grads[nm], deltas[nm], new_m[nm], new_v[nm] = big[nm]
    order = ["c_ctx", "w_mod", "b_mod", "norm_pre1", "norm_post1", "norm_pre2", "norm_post2", "w_in", "hg_lb",
             "hg_onorm", "gla_w_gk", "gla_b_gk", "gla_onorm", "w_br_hg", "w_br_gla", "w_out", "w_ff_gate", "w_ff_up",
             "w_ff_down"]
    return (loss, grad_x[None], *[grads[n] for n in order], *[deltas[n] for n in order],
            *[new_m[n] for n in order], *[new_v[n] for n in order])
```

```python
import functools

import jax
import jax.numpy as jnp
from jax import lax
from jax.experimental import pallas as pl
from jax.experimental.pallas import tpu as pltpu

F32 = jnp.float32
BF16 = jnp.bfloat16
MESH = pl.DeviceIdType.MESH

EPS = 1e-6
CHUNK = 64
SUB = 16
NSUB = CHUNK // SUB
NH = 4
HD = 128
HW = NH * HD
RANK = 16
GATE_NORM = 16.0
N_MOD = 6
TM = 256
BWD_LANES = 8
N_DEV = 8
N_CHIP = 4
VMEM_LIMIT = 56 * 1024 * 1024

ADAM_LR = 0.001
ADAM_B1 = 0.9
ADAM_B2 = 0.999
ADAM_EPS = 1e-08
ADAM_WD = 0.01
ADAM_STEP = 10

VMEM_SPEC = pl.BlockSpec(memory_space=pltpu.VMEM)
ANY_SPEC = pl.BlockSpec(memory_space=pl.ANY)
HBM_SPEC = pl.BlockSpec(memory_space=pltpu.HBM)
SEM_SPEC = pl.BlockSpec(memory_space=pltpu.SEMAPHORE)
EFFECT = pltpu.SideEffectType.DATAFLOW_SIDE_EFFECTING


def _cparams(**kw):
    return pltpu.CompilerParams(vmem_limit_bytes=VMEM_LIMIT, **kw)


def _dot(a, b):
    return jnp.dot(a.astype(BF16), b.astype(BF16), preferred_element_type=F32)


def _dot_nt(a, b):
    return lax.dot_general(a.astype(BF16), b.astype(BF16), (((1,), (1,)), ((), ())), preferred_element_type=F32)


def _dot_tn(a, b):
    return lax.dot_general(a.astype(BF16), b.astype(BF16), (((0,), (0,)), ((), ())), preferred_element_type=F32)


def _sigmoid(x):
    return 1.0 / (1.0 + jnp.exp(-x))


def _silu(x):
    return x * _sigmoid(x)


def _dsilu(x):
    s = _sigmoid(x)
    return s * (1.0 + x * (1.0 - s))


def _log_sigmoid(x):
    return jnp.minimum(x, 0.0) - jnp.log(1.0 + jnp.exp(-jnp.abs(x)))


def _colsum(a):
    return jnp.sum(a, axis=0, keepdims=True)


def _rms(a):
    r = lax.rsqrt(jnp.mean(a * a, axis=-1, keepdims=True) + EPS)
    return a * r, r


def _rms_bwd(dn, n, r):
    return r * (dn - n * jnp.mean(dn * n, axis=-1, keepdims=True))


def _place():
    x, y, c = lax.axis_index("x"), lax.axis_index("y"), lax.axis_index("c")
    chips = [(1 - x, y), (x, 1 - y), (1 - x, 1 - y)]
    return x, y, c, chips


def _allgather8(v, name, after=()):
    rows, cols = v.shape
    n_after = len(after)

    def body(x_ref, *rest):
        out_ref, send_sems, recv_sems, local_sem = rest[n_after:]
        x, y, c, chips = _place()
        me, sibling = (x, y, c), (x, y, 1 - c)

        def blk(px, py, pc):
            return out_ref.at[4 * px + 2 * py + pc]

        def copy(k, block, to, src=None):
            return pltpu.make_async_remote_copy(
                src_ref=blk(*block) if src is None else src, dst_ref=blk(*block),
                send_sem=send_sems.at[k], recv_sem=recv_sems.at[k], device_id=to, device_id_type=MESH)

        mine = pltpu.make_async_copy(x_ref, blk(*me), local_sem)
        mine.start()
        first = [copy(0, me, sibling, src=x_ref)]
        first += [copy(1 + j, me, (*chip, c), src=x_ref) for j, chip in enumerate(chips)]
        for cp in first:
            cp.start()
        passed = [copy(4 + j, (*chip, c), sibling) for j, chip in enumerate(chips)]
        for j, chip in enumerate(chips):
            copy(1 + j, (*chip, c), me).wait_recv()
            passed[j].start()
        copy(0, sibling, me).wait_recv()
        for j, chip in enumerate(chips):
            copy(4 + j, (*chip, 1 - c), me).wait_recv()
        for cp in first + passed:
            cp.wait_send()
        mine.wait()

    return pl.pallas_call(
        body, name=name,
        out_shape=jax.ShapeDtypeStruct((N_DEV, rows, cols), v.dtype),
        in_specs=[VMEM_SPEC] + [ANY_SPEC] * n_after, out_specs=VMEM_SPEC,
        scratch_shapes=[pltpu.SemaphoreType.DMA((7,)), pltpu.SemaphoreType.DMA((7,)), pltpu.SemaphoreType.DMA],
    )(v, *after)


def _allgather8_start(v, name):
    rows, cols = v.shape

    def body(x_ref, out_ref, *rest):
        send_sems, recv_sems, token, local_sem = rest[:4], rest[4:8], rest[-2], rest[-1]
        x, y, c, chips = _place()
        own = out_ref.at[4 * x + 2 * y + c]
        mine = pltpu.make_async_copy(x_ref, own, local_sem)
        mine.start()
        mine.wait()
        for k, to in enumerate([(x, y, 1 - c)] + [(*chip, c) for chip in chips]):
            pltpu.make_async_remote_copy(src_ref=own, dst_ref=own, send_sem=send_sems[k], recv_sem=recv_sems[k],
                                         device_id=to, device_id_type=MESH).start()
        token[...] = jnp.zeros_like(token)

    land = _hbm(lax.empty((N_DEV, rows, cols), v.dtype))
    out = pl.pallas_call(
        body, name=name,
        out_shape=(*[pltpu.SemaphoreType.DMA(())] * 8, pltpu.HBM(land.shape, land.dtype),
                   jax.ShapeDtypeStruct((8, 128), F32)),
        in_specs=[VMEM_SPEC, HBM_SPEC],
        out_specs=(*[SEM_SPEC] * 8, HBM_SPEC, VMEM_SPEC),
        input_output_aliases={1: 8},
        scratch_shapes=[pltpu.SemaphoreType.DMA],
        compiler_params=pltpu.CompilerParams(has_side_effects=EFFECT),
    )(v, land)
    return list(out[:8]), out[8], out[9]


def _allgather8_wait(sems, land, after, name):
    def blk(ref, px, py, pc):
        return ref.at[4 * px + 2 * py + pc]

    def wait_body(out_ref, *rest):
        send_sems, recv_sems = rest[:4], rest[4:8]
        x, y, c, chips = _place()
        me = (x, y, c)
        for k, peer in enumerate([(x, y, 1 - c)] + [(*chip, c) for chip in chips]):
            sent = pltpu.make_async_remote_copy(
                src_ref=blk(out_ref, *me), dst_ref=blk(out_ref, *me), send_sem=send_sems[k], recv_sem=recv_sems[k],
                device_id=peer, device_id_type=MESH)
            sent.wait_send()
            pltpu.make_async_remote_copy(
                src_ref=blk(out_ref, *peer), dst_ref=blk(out_ref, *peer), send_sem=send_sems[k],
                recv_sem=recv_sems[k], device_id=me, device_id_type=MESH).wait_recv()

    def pass_body(out_ref, _, send_sems, recv_sems):
        x, y, c, chips = _place()
        started = []
        for j, chip in enumerate(chips):
            cp = pltpu.make_async_remote_copy(
                src_ref=blk(out_ref, *chip, c), dst_ref=blk(out_ref, *chip, c), send_sem=send_sems.at[j],
                recv_sem=recv_sems.at[j], device_id=(x, y, 1 - c), device_id_type=MESH)
            cp.start()
            started.append(cp)
        for j, chip in enumerate(chips):
            pltpu.make_async_remote_copy(
                src_ref=blk(out_ref, *chip, 1 - c), dst_ref=blk(out_ref, *chip, 1 - c), send_sem=send_sems.at[j],
                recv_sem=recv_sems.at[j], device_id=(x, y, c), device_id_type=MESH).wait_recv()
        for cp in started:
            cp.wait_send()

    land = pl.pallas_call(
        wait_body, name=name,
        out_shape=pltpu.HBM(land.shape, land.dtype),
        in_specs=[HBM_SPEC] + [SEM_SPEC] * 8 + [ANY_SPEC] * len(after),
        out_specs=HBM_SPEC,
        input_output_aliases={0: 0},
        compiler_params=pltpu.CompilerParams(has_side_effects=EFFECT),
    )(land, *sems, *after)
    return pl.pallas_call(
        pass_body, name=name + "_pass",
        out_shape=jax.ShapeDtypeStruct(land.shape, land.dtype),
        in_specs=[ANY_SPEC], out_specs=ANY_SPEC,
        input_output_aliases={0: 0},
        scratch_shapes=[pltpu.SemaphoreType.DMA((3,)), pltpu.SemaphoreType.DMA((3,))],
    )(land)


def _cast_into_blocks(chip_arr, w, name):
    rows, cols = w.shape
    tr = _row_tile(rows, 16, 256)

    def body(chip_ref, w_ref, o_ref):
        o_ref[0] = w_ref[...].astype(BF16)

    return pl.pallas_call(
        body, name=name,
        grid_spec=pltpu.PrefetchScalarGridSpec(
            num_scalar_prefetch=1, grid=(rows // tr,),
            in_specs=[pl.BlockSpec((tr, cols), lambda i, chip_ref: (i, 0))],
            out_specs=pl.BlockSpec((1, tr, cols), lambda i, chip_ref: (chip_ref[0], i, 0))),
        out_shape=jax.ShapeDtypeStruct((N_CHIP, rows, cols), BF16),
        compiler_params=_cparams(dimension_semantics=("parallel",)),
    )(chip_arr, w)


def _halved_by_rows(shape):
    return (shape[1] // 2) % 16 == 0


def _half_of(ref, pc, block=None):
    lead = slice(None) if block is None else block
    if _halved_by_rows(ref.shape):
        h = ref.shape[1] // 2
        return ref.at[lead, pl.ds(pl.multiple_of(pc * h, 16), h), :]
    h = ref.shape[2] // 2
    return ref.at[lead, :, pl.ds(pl.multiple_of(pc * h, 128), h)]


def _half_shape(shape):
    return (shape[0], shape[1] // 2, shape[2]) if _halved_by_rows(shape) else (shape[0], shape[1], shape[2] // 2)


def _half_rows(ref, chip_id, pc):
    return _half_of(ref, pc, chip_id)


def _hbm(a):
    return pltpu.with_memory_space_constraint(a, pltpu.HBM)


def _blocks_start(lands, name, after=()):
    n = len(lands)
    n_sem = 3 * n
    first = n + len(after)

    def body(*refs):
        lnd = refs[:n]
        send_sems, recv_sems = refs[first:first + n_sem], refs[first + n_sem:first + 2 * n_sem]
        token = refs[-1]
        x, y, c, chips = _place()
        me_chip = 2 * x + y
        for k in range(n):
            for j, chip in enumerate(chips):
                pltpu.make_async_remote_copy(
                    src_ref=_half_rows(lnd[k], me_chip, c), dst_ref=_half_rows(lnd[k], me_chip, c),
                    send_sem=send_sems[3 * k + j], recv_sem=recv_sems[3 * k + j],
                    device_id=(*chip, c), device_id_type=MESH).start()
        token[...] = jnp.zeros_like(token)

    out = pl.pallas_call(
        body, name=name,
        out_shape=(*[pltpu.SemaphoreType.DMA(())] * (2 * n_sem),
                   *[pltpu.HBM(l.shape, l.dtype) for l in lands],
                   jax.ShapeDtypeStruct((8, 128), F32)),
        in_specs=[HBM_SPEC] * n + [ANY_SPEC] * len(after),
        out_specs=(*[SEM_SPEC] * (2 * n_sem), *[HBM_SPEC] * n, VMEM_SPEC),
        input_output_aliases={i: 2 * n_sem + i for i in range(n)},
        compiler_params=pltpu.CompilerParams(has_side_effects=EFFECT),
    )(*[_hbm(l) for l in lands], *after)
    return list(out[:2 * n_sem]), list(out[2 * n_sem:2 * n_sem + n]), out[-1]


def _blocks_wait(sems, lands, after, name):
    n = len(lands)
    n_sem = 3 * n

    def body(*refs):
        lnd = refs[:n]
        s_sems, r_sems = refs[n:n + n_sem], refs[n + n_sem:n + 2 * n_sem]
        x, y, c, chips = _place()
        me_chip = 2 * x + y
        for k in range(n):
            for j, (px, py) in enumerate(chips):
                cp = pltpu.make_async_remote_copy(
                    src_ref=_half_rows(lnd[k], me_chip, c), dst_ref=_half_rows(lnd[k], 2 * px + py, c),
                    send_sem=s_sems[3 * k + j], recv_sem=r_sems[3 * k + j],
                    device_id=(px, py, c), device_id_type=MESH)
                cp.wait_send()
                cp.wait_recv()

    out = pl.pallas_call(
        body, name=name,
        out_shape=tuple(pltpu.HBM(l.shape, l.dtype) for l in lands),
        in_specs=[HBM_SPEC] * n + [SEM_SPEC] * (2 * n_sem) + [ANY_SPEC] * len(after),
        out_specs=[HBM_SPEC] * n,
        input_output_aliases={i: i for i in range(n)},
        compiler_params=pltpu.CompilerParams(has_side_effects=EFFECT),
    )(*lands, *sems, *after)
    return list(out)


def _forward_start(lands, name):
    n = len(lands)
    n_sem = 3 * n

    def body(*refs):
        lnd = refs[:n]
        send_sems, recv_sems = refs[n:n + n_sem], refs[n + n_sem:n + 2 * n_sem]
        x, y, c, chips = _place()
        for k in range(n):
            for j, (px, py) in enumerate(chips):
                pltpu.make_async_remote_copy(
                    src_ref=_half_rows(lnd[k], 2 * px + py, c), dst_ref=_half_rows(lnd[k], 2 * px + py, c),
                    send_sem=send_sems[3 * k + j], recv_sem=recv_sems[3 * k + j],
                    device_id=(x, y, 1 - c), device_id_type=MESH).start()
        refs[-1][...] = jnp.zeros_like(refs[-1])

    out = pl.pallas_call(
        body, name=name,
        out_shape=(*[pltpu.SemaphoreType.DMA(())] * (2 * n_sem), *[pltpu.HBM(l.shape, l.dtype) for l in lands],
                   jax.ShapeDtypeStruct((8, 128), F32)),
        in_specs=[HBM_SPEC] * n,
        out_specs=(*[SEM_SPEC] * (2 * n_sem), *[HBM_SPEC] * n, VMEM_SPEC),
        input_output_aliases={i: 2 * n_sem + i for i in range(n)},
        compiler_params=pltpu.CompilerParams(has_side_effects=EFFECT),
    )(*[_hbm(l) for l in lands])
    return list(out[:2 * n_sem]), list(out[2 * n_sem:2 * n_sem + n]), out[-1]


def _forward_wait(sems, lands, after, name):
    n = len(lands)
    n_sem = 3 * n

    def body(*refs):
        lnd = refs[:n]
        s_sems, r_sems = refs[n:n + n_sem], refs[n + n_sem:n + 2 * n_sem]
        x, y, c, chips = _place()
        for k in range(n):
            for j, (px, py) in enumerate(chips):
                cp = pltpu.make_async_remote_copy(
                    src_ref=_half_rows(lnd[k], 2 * px + py, c), dst_ref=_half_rows(lnd[k], 2 * px + py, 1 - c),
                    send_sem=s_sems[3 * k + j], recv_sem=r_sems[3 * k + j],
                    device_id=(x, y, 1 - c), device_id_type=MESH)
                cp.wait_send()
                cp.wait_recv()

    out = pl.pallas_call(
        body, name=name,
        out_shape=tuple(pltpu.HBM(l.shape, l.dtype) for l in lands),
        in_specs=[HBM_SPEC] * n + [SEM_SPEC] * (2 * n_sem) + [ANY_SPEC] * len(after),
        out_specs=[HBM_SPEC] * n,
        input_output_aliases={i: i for i in range(n)},
        compiler_params=pltpu.CompilerParams(has_side_effects=EFFECT),
    )(*lands, *sems, *after)
    return list(out)


def _blocks_finish(lands, name):
    n = len(lands)

    def body(*refs):
        lnd = refs[n:2 * n]
        send_sems, recv_sems = refs[2 * n:]
        x, y, c, chips = _place()
        sibling = (x, y, 1 - c)

        def copy(k, j, chip_id, pc):
            return pltpu.make_async_remote_copy(
                src_ref=_half_rows(lnd[k], chip_id, pc), dst_ref=_half_rows(lnd[k], chip_id, pc),
                send_sem=send_sems.at[k, j], recv_sem=recv_sems.at[k, j], device_id=sibling, device_id_type=MESH)

        started = []
        for k in range(n):
            for j, (px, py) in enumerate(chips):
                cp = copy(k, j, 2 * px + py, c)
                cp.start()
                started.append(cp)
        for k in range(n):
            for j, (px, py) in enumerate(chips):
                copy(k, j, 2 * px + py, 1 - c).wait_recv()
        for cp in started:
            cp.wait_send()

    out = pl.pallas_call(
        body, name=name,
        out_shape=[jax.ShapeDtypeStruct(l.shape, l.dtype) for l in lands],
        in_specs=[ANY_SPEC] * n, out_specs=[ANY_SPEC] * n,
        input_output_aliases={i: i for i in range(n)},
        scratch_shapes=[pltpu.SemaphoreType.DMA((n, 3)), pltpu.SemaphoreType.DMA((n, 3))],
    )(*lands)
    return list(out)


def _send_half_start(arrs, name, after=()):
    n = len(arrs)
    first = 2 * n + len(after)

    def body(*refs):
        ins, lnd = refs[:n], refs[n:2 * n]
        send_sems, recv_sems = refs[first:first + n], refs[first + n:first + 2 * n]
        token = refs[-1]
        x, y, c, _ = _place()
        for k in range(n):
            pltpu.make_async_remote_copy(
                src_ref=_half_of(ins[k], 1 - c), dst_ref=lnd[k], send_sem=send_sems[k], recv_sem=recv_sems[k],
                device_id=(x, y, 1 - c), device_id_type=MESH).start()
        token[...] = jnp.zeros_like(token)

    lands = [_hbm(lax.empty(_half_shape(a.shape), a.dtype)) for a in arrs]
    out = pl.pallas_call(
        body, name=name,
        out_shape=(*[pltpu.SemaphoreType.DMA(())] * (2 * n), *[pltpu.HBM(a.shape, a.dtype) for a in arrs],
                   *[pltpu.HBM(l.shape, l.dtype) for l in lands], jax.ShapeDtypeStruct((8, 128), F32)),
        in_specs=[HBM_SPEC] * (2 * n) + [ANY_SPEC] * len(after),
        out_specs=(*[SEM_SPEC] * (2 * n), *[HBM_SPEC] * (2 * n), VMEM_SPEC),
        input_output_aliases={i: 2 * n + i for i in range(2 * n)},
        compiler_params=pltpu.CompilerParams(has_side_effects=EFFECT),
    )(*[_hbm(a) for a in arrs], *lands, *after)
    return list(out[:2 * n]), list(out[2 * n:3 * n]), list(out[3 * n:4 * n]), out[-1]


def _send_half_wait(sems, arrs, lands, after, name):
    n = len(arrs)

    def body(*refs):
        ins, lnd = refs[:n], refs[n:2 * n]
        s_sems, r_sems = refs[2 * n:3 * n], refs[3 * n:4 * n]
        x, y, c, _ = _place()
        for k in range(n):
            cp = pltpu.make_async_remote_copy(
                src_ref=_half_of(ins[k], 1 - c), dst_ref=lnd[k], send_sem=s_sems[k], recv_sem=r_sems[k],
                device_id=(x, y, 1 - c), device_id_type=MESH)
            cp.wait_send()
            cp.wait_recv()

    out = pl.pallas_call(
        body, name=name,
        out_shape=tuple(pltpu.HBM(a.shape, a.dtype) for a in list(arrs) + list(lands)),
        in_specs=[HBM_SPEC] * (2 * n) + [SEM_SPEC] * (2 * n) + [ANY_SPEC] * len(after),
        out_specs=[HBM_SPEC] * (2 * n),
        input_output_aliases={i: i for i in range(2 * n)},
        compiler_params=pltpu.CompilerParams(has_side_effects=EFFECT),
    )(*arrs, *lands, *sems, *after)
    return list(out[:n]), list(out[n:])


def _scatter_start(arrs, name, after=()):
    n = len(arrs)
    n_sem = 3 * n
    first = 2 * n + len(after)

    def body(*refs):
        ins, lnd = refs[:n], refs[n:2 * n]
        send_sems, recv_sems = refs[first:first + n_sem], refs[first + n_sem:first + 2 * n_sem]
        token = refs[-1]
        x, y, c, chips = _place()
        me_chip = 2 * x + y
        for k in range(n):
            for j, (px, py) in enumerate(chips):
                pltpu.make_async_remote_copy(
                    src_ref=ins[k].at[2 * px + py], dst_ref=lnd[k].at[me_chip],
                    send_sem=send_sems[3 * k + j], recv_sem=recv_sems[3 * k + j],
                    device_id=(px, py, c), device_id_type=MESH).start()
        token[...] = jnp.zeros_like(token)

    lands = [_hbm(lax.empty(a.shape, a.dtype)) for a in arrs]
    out = pl.pallas_call(
        body, name=name,
        out_shape=(*[pltpu.SemaphoreType.DMA(())] * (2 * n_sem),
                   *[pltpu.HBM(a.shape, a.dtype) for a in arrs], *[pltpu.HBM(a.shape, a.dtype) for a in arrs],
                   jax.ShapeDtypeStruct((8, 128), F32)),
        in_specs=[HBM_SPEC] * (2 * n) + [ANY_SPEC] * len(after),
        out_specs=(*[SEM_SPEC] * (2 * n_sem), *[HBM_SPEC] * (2 * n), VMEM_SPEC),
        input_output_aliases={i: 2 * n_sem + i for i in range(2 * n)},
        compiler_params=pltpu.CompilerParams(has_side_effects=EFFECT),
    )(*[_hbm(a) for a in arrs], *lands, *after)
    base = 2 * n_sem
    return list(out[:base]), list(out[base:base + n]), list(out[base + n:base + 2 * n]), out[-1]


def _scatter_wait(sems, arrs, lands, after, name):
    n = len(arrs)
    n_sem = 3 * n

    def body(*refs):
        ins, lnd = refs[:n], refs[n:2 * n]
        s_sems, r_sems = refs[2 * n:2 * n + n_sem], refs[2 * n + n_sem:2 * n + 2 * n_sem]
        x, y, c, chips = _place()
        for k in range(n):
            for j, (px, py) in enumerate(chips):
                cp = pltpu.make_async_remote_copy(
                    src_ref=ins[k].at[2 * px + py], dst_ref=lnd[k].at[2 * px + py],
                    send_sem=s_sems[3 * k + j], recv_sem=r_sems[3 * k + j],
                    device_id=(px, py, c), device_id_type=MESH)
                cp.wait_send()
                cp.wait_recv()

    out = pl.pallas_call(
        body, name=name,
        out_shape=tuple(pltpu.HBM(a.shape, a.dtype) for a in list(arrs) + list(lands)),
        in_specs=[HBM_SPEC] * (2 * n) + [SEM_SPEC] * (2 * n_sem) + [ANY_SPEC] * len(after),
        out_specs=[HBM_SPEC] * (2 * n),
        input_output_aliases={i: i for i in range(2 * n)},
        compiler_params=pltpu.CompilerParams(has_side_effects=EFFECT),
    )(*arrs, *lands, *sems, *after)
    return list(out[:n]), list(out[n:])


def _sum_owner(chip_arr, pairs, got, name):
    nb, h, cols = got.shape
    tr = _row_tile(h, 16, 256)

    def body(chip_ref, own_ref, a_ref, b_ref, c_ref, o_ref):
        o_ref[...] = ((own_ref[0].astype(F32) + a_ref[0].astype(F32)) + b_ref[0].astype(F32)) + c_ref[0].astype(F32)

    def slot(off):
        return pl.BlockSpec((1, tr, cols), lambda i, chip_ref: ((chip_ref[0] + off) % N_CHIP, i, 0))

    return pl.pallas_call(
        body, name=name,
        grid_spec=pltpu.PrefetchScalarGridSpec(
            num_scalar_prefetch=1, grid=(h // tr,),
            in_specs=[slot(0), slot(1), slot(2), slot(3)],
            out_specs=pl.BlockSpec((tr, cols), lambda i, chip_ref: (i, 0))),
        out_shape=jax.ShapeDtypeStruct((h, cols), F32),
        compiler_params=_cparams(dimension_semantics=("parallel",)),
    )(chip_arr, pairs, got, got, got)


def _swap_start(arrs, name, after=()):
    n = len(arrs)
    first = 2 * n + len(after)

    def body(*refs):
        ins, lnd = refs[:n], refs[n:2 * n]
        send_sems, recv_sems = refs[first:first + n], refs[first + n:first + 2 * n]
        x, y, c, _ = _place()
        for k in range(n):
            pltpu.make_async_remote_copy(
                src_ref=ins[k], dst_ref=lnd[k], send_sem=send_sems[k], recv_sem=recv_sems[k],
                device_id=(x, y, 1 - c), device_id_type=MESH).start()

    lands = [_hbm(lax.empty(a.shape, a.dtype)) for a in arrs]
    out = pl.pallas_call(
        body, name=name,
        out_shape=(*[pltpu.SemaphoreType.DMA(())] * (2 * n), *[pltpu.HBM(a.shape, a.dtype) for a in arrs],
                   *[pltpu.HBM(a.shape, a.dtype) for a in arrs]),
        in_specs=[HBM_SPEC] * (2 * n) + [ANY_SPEC] * len(after),
        out_specs=(*[SEM_SPEC] * (2 * n), *[HBM_SPEC] * (2 * n)),
        input_output_aliases={i: 2 * n + i for i in range(2 * n)},
        compiler_params=pltpu.CompilerParams(has_side_effects=EFFECT),
    )(*[_hbm(a) for a in arrs], *lands, *after)
    return list(out[:2 * n]), list(out[2 * n:3 * n]), list(out[3 * n:4 * n])


def _swap_wait(sems, arrs, lands, after, name):
    n = len(arrs)

    def body(*refs):
        ins, lnd = refs[:n], refs[n:2 * n]
        s_sems, r_sems = refs[2 * n:3 * n], refs[3 * n:4 * n]
        x, y, c, _ = _place()
        for k in range(n):
            cp = pltpu.make_async_remote_copy(
                src_ref=ins[k], dst_ref=lnd[k], send_sem=s_sems[k], recv_sem=r_sems[k],
                device_id=(x, y, 1 - c), device_id_type=MESH)
            cp.wait_send()
            cp.wait_recv()

    out = pl.pallas_call(
        body, name=name,
        out_shape=tuple(pltpu.HBM(a.shape, a.dtype) for a in list(arrs) + list(lands)),
        in_specs=[HBM_SPEC] * (2 * n) + [SEM_SPEC] * (2 * n) + [ANY_SPEC] * len(after),
        out_specs=[HBM_SPEC] * (2 * n),
        input_output_aliases={i: i for i in range(2 * n)},
        compiler_params=pltpu.CompilerParams(has_side_effects=EFFECT),
    )(*arrs, *lands, *sems, *after)
    return list(out[:n]), list(out[n:])


def _row_tile(h, mult=8, cap=128):
    for t in range(cap - cap % mult, mult - 1, -mult):
        if h % t == 0:
            return t
    if mult > 8:
        return _row_tile(h, 8, cap)
    raise ValueError(h)


def _pair_sum(c_arr, full, recv, name):
    nb, rows, cols = full.shape

    def body(c_ref, f_ref, r_ref, o_ref):
        o_ref[...] = (f_ref[...] + r_ref[...]).astype(BF16)

    if _halved_by_rows(full.shape):
        h = rows // 2
        tr = _row_tile(h, 16, 256)
        steps = h // tr
        own = pl.BlockSpec((1, tr, cols), lambda b, i, c_ref: (b, c_ref[0] * steps + i, 0))
        half = pl.BlockSpec((1, tr, cols), lambda b, i, c_ref: (b, i, 0))
    else:
        steps = 1
        own = pl.BlockSpec((1, rows, cols // 2), lambda b, i, c_ref: (b, 0, c_ref[0]))
        half = pl.BlockSpec((1, rows, cols // 2), lambda b, i, c_ref: (b, 0, 0))
    return pl.pallas_call(
        body, name=name,
        grid_spec=pltpu.PrefetchScalarGridSpec(
            num_scalar_prefetch=1, grid=(nb, steps), in_specs=[own, half], out_specs=half),
        out_shape=jax.ShapeDtypeStruct(_half_shape(full.shape), BF16),
        compiler_params=_cparams(dimension_semantics=("parallel", "parallel")),
    )(c_arr, full, recv)


def _adam_math(g, w, m, v):
    m1 = ADAM_B1 * m + (1.0 - ADAM_B1) * g
    v1 = ADAM_B2 * v + (1.0 - ADAM_B2) * (g * g)
    m_hat = m1 / (1.0 - ADAM_B1 ** ADAM_STEP)
    v_hat = v1 / (1.0 - ADAM_B2 ** ADAM_STEP)
    delta = -ADAM_LR * (m_hat / (jnp.sqrt(v_hat) + ADAM_EPS) + ADAM_WD * w)
    return delta, m1, v1


def _adamw_halves(c_arr, own, other, w, m, v, name):
    rows, cols = w.shape
    by_rows = own.shape[1] == cols

    def body(c_ref, own_ref, oth_ref, w_ref, m_ref, v_ref, g_out, d_out, m_out, v_out):
        if by_rows:
            g = jnp.where(pl.program_id(0) == c_ref[0], own_ref[...], oth_ref[...])
        else:
            own_, oth_ = own_ref[...], oth_ref[...]
            g = jnp.where(c_ref[0] == 0, jnp.concatenate([own_, oth_], axis=1), jnp.concatenate([oth_, own_], axis=1))
        d, m1, v1 = _adam_math(g, w_ref[...], m_ref[...], v_ref[...])
        g_out[...] = g
        d_out[...] = d
        m_out[...] = m1
        v_out[...] = v1

    if by_rows:
        h = rows // 2
        tr = _row_tile(h)
        steps = h // tr
        grid = (2, steps)
        half_spec = pl.BlockSpec((tr, cols), lambda p, i, c_ref: (i, 0))
        full_spec = pl.BlockSpec((tr, cols), lambda p, i, c_ref: (p * steps + i, 0))
    else:
        tr = _row_tile(rows)
        grid = (1, rows // tr)
        half_spec = pl.BlockSpec((tr, cols // 2), lambda p, i, c_ref: (i, 0))
        full_spec = pl.BlockSpec((tr, cols), lambda p, i, c_ref: (i, 0))
    return pl.pallas_call(
        body, name=name,
        grid_spec=pltpu.PrefetchScalarGridSpec(
            num_scalar_prefetch=1, grid=grid,
            in_specs=[half_spec, half_spec, full_spec, full_spec, full_spec],
            out_specs=[full_spec] * 4),
        out_shape=[jax.ShapeDtypeStruct(w.shape, F32)] * 4,
        compiler_params=_cparams(dimension_semantics=("parallel", "parallel")),
    )(c_arr, own, other, w, m, v)


def _adamw_whole(items, name):
    n = len(items)

    def body(*refs):
        ins, outs = refs[:4 * n], refs[4 * n:]
        for k in range(n):
            g, w, m, v = (r[...] for r in ins[4 * k:4 * k + 4])
            d, m1, v1 = _adam_math(g, w, m, v)
            outs[3 * k][...] = d
            outs[3 * k + 1][...] = m1
            outs[3 * k + 2][...] = v1

    flat = [a for it in items for a in it]
    shapes = [jax.ShapeDtypeStruct(it[1].shape, F32) for it in items for _ in range(3)]
    out = pl.pallas_call(
        body, name=name, out_shape=shapes,
        in_specs=[VMEM_SPEC] * (4 * n), out_specs=[VMEM_SPEC] * (3 * n),
        compiler_params=_cparams(),
    )(*flat)
    return [tuple(out[3 * k:3 * k + 3]) for k in range(n)]


def _adamw_tiled(g, w, m, v, name):
    rows, cols = w.shape
    tr = _row_tile(rows)

    def body(g_ref, w_ref, m_ref, v_ref, d_out, m_out, v_out):
        d, m1, v1 = _adam_math(g_ref[...], w_ref[...], m_ref[...], v_ref[...])
        d_out[...] = d
        m_out[...] = m1
        v_out[...] = v1

    spec = pl.BlockSpec((tr, cols), lambda i: (i, 0))
    return pl.pallas_call(
        body, name=name, grid=(rows // tr,),
        out_shape=[jax.ShapeDtypeStruct(w.shape, F32)] * 3,
        in_specs=[spec] * 4, out_specs=[spec] * 3,
        compiler_params=_cparams(dimension_semantics=("parallel",)),
    )(g, w, m, v)


def _mod_forward(cond, w_mod, b_mod_cols, name):
    def body(c_ref, w_ref, b_ref, o_ref):
        o_ref[...] = _dot(_silu(c_ref[...]), w_ref[...]) + b_ref[...]

    return pl.pallas_call(
        body, name=name, out_shape=jax.ShapeDtypeStruct((cond.shape[0], w_mod.shape[1]), F32),
        in_specs=[VMEM_SPEC] * 3, out_specs=VMEM_SPEC, compiler_params=_cparams(),
    )(cond, w_mod, b_mod_cols)


def _mod_backward(cond, w_mod, dmod_cols, name):
    def body(c_ref, w_ref, d_ref, gw_ref, gc_ref):
        s = _silu(c_ref[...])
        d = d_ref[...]
        gw_ref[...] = _dot_tn(s, d)
        gc_ref[...] = _dot_nt(d[8:16, :], w_ref[...])

    return pl.pallas_call(
        body, name=name,
        out_shape=[jax.ShapeDtypeStruct(w_mod.shape, F32), jax.ShapeDtypeStruct((8, w_mod.shape[0]), F32)],
        in_specs=[VMEM_SPEC] * 3, out_specs=[VMEM_SPEC] * 2, compiler_params=_cparams(),
    )(cond, w_mod, dmod_cols)


def _col_chunks(width, step=512):
    return [(s, min(step, width - s)) for s in range(0, width, step)]


def _w_in_row(p_off):
    if p_off < 9 * HW:
        return p_off
    return 9 * HW if p_off == OFF_LR else p_off + 2 * RANK


def _in_projection(ctx0, x0, modc, modx, pre1, w_t, n_ctx_tiles, name):
    d = x0.shape[1]
    rows = ctx0.shape[0] + x0.shape[0]
    width = P_WIDTH

    def body(ctx_ref, x_ref, modc_ref, modx_ref, pre_ref, w_ref, h_ref, p_ref):
        is_ctx = pl.program_id(0) < n_ctx_tiles
        n, _ = _rms(jnp.where(is_ctx, ctx_ref[...], x_ref[...]))
        shift = jnp.where(is_ctx, modc_ref[0:1, :], modx_ref[0:1, :])
        scale = jnp.where(is_ctx, modc_ref[1:2, :], modx_ref[1:2, :])
        h = (n * pre_ref[...] * (1.0 + scale) + shift).astype(BF16)
        h_ref[...] = h
        for s, w in _col_chunks(width):
            p_ref[:, s:s + w] = _dot_nt(h, w_ref[_w_in_row(s):_w_in_row(s) + w, :])

    row = lambda i: (i, 0)
    fixed = lambda i: (0, 0)
    return pl.pallas_call(
        body, name=name, grid=(rows // TM,),
        out_shape=[jax.ShapeDtypeStruct((rows, d), BF16), jax.ShapeDtypeStruct((rows, width), F32)],
        in_specs=[pl.BlockSpec((TM, d), lambda i: (jnp.minimum(i, n_ctx_tiles - 1), 0)),
                  pl.BlockSpec((TM, d), lambda i: (jnp.maximum(i - n_ctx_tiles, 0), 0)),
                  pl.BlockSpec((8, d), fixed), pl.BlockSpec((8, d), fixed), pl.BlockSpec((1, d), fixed), VMEM_SPEC],
        out_specs=[pl.BlockSpec((TM, d), row), pl.BlockSpec((TM, width), row)],
        compiler_params=_cparams(dimension_semantics=("parallel",)),
    )(ctx0, x0, modc, modx, pre1, w_t)


C_HQ, C_HI, C_HF_FW, C_HF_BW, C_HGATE, C_GQ, C_GK, C_GV, C_GGATE = range(9)
OFF_GATE_HG = 9 * HW
OFF_LR = 13 * HW
P_WIDTH = OFF_LR + 128


def _head_norm_fwd(o, w):
    outs, ns, rs = [], [], []
    for h in range(NH):
        n, r = _rms(o[:, h * HD:(h + 1) * HD])
        ns.append(n)
        rs.append(r)
        outs.append(n * w)
    return jnp.concatenate(outs, axis=1), ns, rs


def _mixer_tail(z, o_hg, o_gla, p_hgate, p_ggate, p_gate_hg, p_gate_gla, hg_on, gla_on, wbh, wbg, wout):
    on_hg, n_hg, r_hg = _head_norm_fwd(o_hg, hg_on)
    on_gla, n_gla, r_gla = _head_norm_fwd(o_gla, gla_on)
    og_hg = (on_hg * _silu(p_hgate)).astype(BF16)
    og_gla = (on_gla * _silu(p_ggate)).astype(BF16)
    b_hg = jnp.dot(og_hg, wbh, preferred_element_type=F32)
    b_gla = jnp.dot(og_gla, wbg, preferred_element_type=F32)
    s_hg = _sigmoid(p_gate_hg)
    s_gla = _sigmoid(p_gate_gla)
    merged = (s_hg * b_hg + s_gla * b_gla).astype(BF16)
    y1 = jnp.dot(merged, wout, preferred_element_type=F32)
    return dict(on_hg=on_hg, n_hg=n_hg, r_hg=r_hg, on_gla=on_gla, n_gla=n_gla, r_gla=r_gla, og_hg=og_hg,
                og_gla=og_gla, b_hg=b_hg, b_gla=b_gla, s_hg=s_hg, s_gla=s_gla, merged=merged, y1=y1)


def _mixer_tail_fwd(x_lat, p, o_list, modx, norms, onorms, w_br_hg, w_br_gla, w_out, n_ctx_tiles, name):
    rows, d = x_lat.shape

    def body(x_ref, ofw_hg, obw_hg, ofw_gla, obw_gla, p_hgate, p_ggate, p_ghg_a, p_ghg_b, p_ggla_a, p_ggla_b,
             modx_ref, norm_ref, on_ref, wbh_ref, wbg_ref, wout_ref, z2_ref, y1_ref, mrg_ref, oghg_ref, oggla_ref):
        p_gate_hg = jnp.concatenate([p_ghg_a[...], p_ghg_b[...]], axis=1)
        p_gate_gla = jnp.concatenate([p_ggla_a[...], p_ggla_b[...]], axis=1)
        t = _mixer_tail(x_ref[...], ofw_hg[...] + obw_hg[...], ofw_gla[...] + obw_gla[...], p_hgate[...],
                        p_ggate[...], p_gate_hg, p_gate_gla, on_ref[0:1, 0:HD], on_ref[1:2, 0:HD],
                        wbh_ref[...], wbg_ref[...], wout_ref[...])
        y1_ref[...] = t["y1"]
        mrg_ref[...] = t["merged"]
        oghg_ref[...] = t["og_hg"]
        oggla_ref[...] = t["og_gla"]
        n1, _ = _rms(t["y1"])
        z2_ref[...] = x_ref[...] + n1 * norm_ref[1:2, :] * modx_ref[2:3, :]

    lat = lambda i: (i, 0)
    full = lambda i: (i + n_ctx_tiles, 0)
    fixed = lambda i: (0, 0)

    def pcol(blk):
        return pl.BlockSpec((TM, HW), lambda i: (i + n_ctx_tiles, blk))

    in_specs = ([pl.BlockSpec((TM, d), lat)] + [pl.BlockSpec((TM, HW), full)] * 4
                + [pcol(C_HGATE), pcol(C_GGATE), pcol(9), pcol(10), pcol(11), pcol(12)]
                + [pl.BlockSpec((8, d), fixed)] * 3 + [VMEM_SPEC] * 3)
    bf = lambda w: jax.ShapeDtypeStruct((rows, w), BF16)
    f32 = jax.ShapeDtypeStruct((rows, d), F32)
    return pl.pallas_call(
        body, name=name, grid=(rows // TM,), out_shape=[f32, f32, bf(d), bf(HW), bf(HW)], in_specs=in_specs,
        out_specs=[pl.BlockSpec((TM, d), lat)] * 3 + [pl.BlockSpec((TM, HW), lat)] * 2,
        compiler_params=_cparams(dimension_semantics=("parallel",)),
    )(x_lat, *o_list, p, p, p, p, p, p, modx, norms, onorms, w_br_hg, w_br_gla, w_out)


def _ffn_fwd_bwd(z2, modx, norms, w_gate, w_up, w_down, target, name):
    rows, d = z2.shape
    dff = w_gate.shape[0]
    inv_d = 1.0 / d

    def body(z2_ref, modx_ref, norm_ref, wg_ref, wu_ref, wd_ref, t_ref,
             loss_ref, dz2_ref, h2_ref, a_ref, du_ref, dv_ref, dy2_ref, stat_ref):
        i = pl.program_id(0)
        pre2, post2 = norm_ref[2:3, :], norm_ref[3:4, :]
        shift2, scale2, gate2 = modx_ref[3:4, :], modx_ref[4:5, :], modx_ref[5:6, :]
        z2 = z2_ref[...]
        n2, r2 = _rms(z2)
        nw2 = n2 * pre2
        h2 = (nw2 * (1.0 + scale2) + shift2).astype(BF16)
        h2_ref[...] = h2
        u = _dot_nt(h2, wg_ref[...])
        v = _dot_nt(h2, wu_ref[...])
        su = _silu(u)
        a = (su * v).astype(BF16)
        a_ref[...] = a
        y2 = jnp.dot(a, wd_ref[...], preferred_element_type=F32)
        n3, r3 = _rms(y2)
        err = z2 + n3 * post2 * gate2 - t_ref[...]
        part = 0.5 * inv_d * jnp.sum(err * err)
        dz3 = err * inv_d
        dgate2 = _colsum(dz3 * n3 * post2)
        tt = dz3 * gate2
        dpost2 = _colsum(tt * n3)
        dy2 = _rms_bwd(tt * post2, n3, r3).astype(BF16)
        dy2_ref[...] = dy2
        da = _dot_nt(dy2, wd_ref[...])
        du = (da * v * _dsilu(u)).astype(BF16)
        dv = (da * su).astype(BF16)
        du_ref[...] = du
        dv_ref[...] = dv
        dh2 = (jnp.dot(du, wg_ref[...], preferred_element_type=F32)
               + jnp.dot(dv, wu_ref[...], preferred_element_type=F32))
        dshift2 = _colsum(dh2)
        dscale2 = _colsum(dh2 * nw2)
        dnw2 = dh2 * (1.0 + scale2)
        dpre2 = _colsum(dnw2 * n2)
        dz2_ref[...] = dz3 + _rms_bwd(dnw2 * pre2, n2, r2)

        @pl.when(i == 0)
        def _():
            stat_ref[...] = jnp.zeros_like(stat_ref)
            loss_ref[...] = jnp.zeros_like(loss_ref)

        for r, val in enumerate((dshift2, dscale2, dgate2, dpre2, dpost2)):
            stat_ref[r:r + 1, :] += val
        loss_ref[...] += part
        stat_ref[5:6, 0:128] += part

    lat = lambda i: (i, 0)
    fixed = lambda i: (0, 0)
    bf = lambda w: jax.ShapeDtypeStruct((rows, w), BF16)
    return pl.pallas_call(
        body, name=name, grid=(rows // TM,),
        out_shape=[jax.ShapeDtypeStruct((8, 128), F32), jax.ShapeDtypeStruct((rows, d), F32), bf(d), bf(dff), bf(dff),
                   bf(dff), bf(d), jax.ShapeDtypeStruct((8, d), F32)],
        in_specs=[pl.BlockSpec((TM, d), lat), pl.BlockSpec((8, d), fixed), pl.BlockSpec((8, d), fixed)]
        + [VMEM_SPEC] * 3 + [pl.BlockSpec((TM, d), lat)],
        out_specs=[pl.BlockSpec((8, 128), fixed), pl.BlockSpec((TM, d), lat), pl.BlockSpec((TM, d), lat),
                   pl.BlockSpec((TM, dff), lat), pl.BlockSpec((TM, dff), lat), pl.BlockSpec((TM, dff), lat),
                   pl.BlockSpec((TM, d), lat), pl.BlockSpec((8, d), fixed)],
        compiler_params=_cparams(dimension_semantics=("arbitrary",)),
    )(z2, modx, norms, w_gate, w_up, w_down, target)


def _mixer_tail_bwd(x_lat, p, o_list, dz2, y1, modx, norms, onorms, w_br_hg, w_br_gla, w_out, n_ctx_tiles, n_tiles,
                    name):
    rows, d = x_lat.shape
    total = n_tiles * TM

    def body(x_ref, ofw_hg, obw_hg, ofw_gla, obw_gla, p_hgate, p_ggate, p_ghg_a, p_ghg_b, p_ggla_a, p_ggla_b,
             dz2_ref, y1_ref, modx_ref, norm_ref, on_ref, wbh_ref, wbg_ref, wout_ref,
             dohg_ref, dogla_ref, dhgate_ref, dggate_ref, dghg_ref, dggla_ref, dy1_ref, dbhg_ref, dbgla_ref,
             stat_ref):
        i = pl.program_id(0)

        @pl.when(i == 0)
        def _():
            stat_ref[...] = jnp.zeros_like(stat_ref)

        @pl.when(i < n_ctx_tiles)
        def _():
            for ref in (dohg_ref, dogla_ref, dhgate_ref, dggate_ref, dghg_ref, dggla_ref):
                ref[...] = jnp.zeros_like(ref)

        @pl.when(i >= n_ctx_tiles)
        def _():
            post1, gate1 = norm_ref[1:2, :], modx_ref[2:3, :]
            hg_on, gla_on = on_ref[0:1, 0:HD], on_ref[1:2, 0:HD]
            p_gate_hg = jnp.concatenate([p_ghg_a[...], p_ghg_b[...]], axis=1)
            p_gate_gla = jnp.concatenate([p_ggla_a[...], p_ggla_b[...]], axis=1)
            ph, pg = p_hgate[...], p_ggate[...]
            t = _mixer_tail(x_ref[...], ofw_hg[...] + obw_hg[...], ofw_gla[...] + obw_gla[...], ph, pg,
                            p_gate_hg, p_gate_gla, hg_on, gla_on, wbh_ref[...], wbg_ref[...], wout_ref[...])
            dz2 = dz2_ref[...]
            n1, r1 = _rms(y1_ref[...])
            dgate1 = _colsum(dz2 * n1 * post1)
            tt = dz2 * gate1
            dpost1 = _colsum(tt * n1)
            dy1 = _rms_bwd(tt * post1, n1, r1).astype(BF16)
            dy1_ref[...] = dy1
            dmerged = _dot_nt(dy1, wout_ref[...])
            dghg_ref[...] = (dmerged * t["b_hg"] * t["s_hg"] * (1.0 - t["s_hg"])).astype(BF16)
            dggla_ref[...] = (dmerged * t["b_gla"] * t["s_gla"] * (1.0 - t["s_gla"])).astype(BF16)
            db_hg = (dmerged * t["s_hg"]).astype(BF16)
            db_gla = (dmerged * t["s_gla"]).astype(BF16)
            dbhg_ref[...] = db_hg
            dbgla_ref[...] = db_gla
            don_acc = []
            for (db, wb, pgate, on, ns, rs, gain, gate_ref, do_ref) in (
                    (db_hg, wbh_ref, ph, t["on_hg"], t["n_hg"], t["r_hg"], hg_on, dhgate_ref, dohg_ref),
                    (db_gla, wbg_ref, pg, t["on_gla"], t["n_gla"], t["r_gla"], gla_on, dggate_ref, dogla_ref)):
                dog = _dot_nt(db, wb[...])
                gate_ref[...] = (dog * on * _dsilu(pgate)).astype(BF16)
                don = dog * _silu(pgate)
                acc = jnp.zeros((1, HD), F32)
                for h in range(NH):
                    sl = slice(h * HD, (h + 1) * HD)
                    acc = acc + _colsum(don[:, sl] * ns[h])
                    do_ref[:, sl] = _rms_bwd(don[:, sl] * gain, ns[h], rs[h]).astype(BF16)
                don_acc.append(acc)
            stat_ref[0:1, :] += dgate1
            stat_ref[1:2, :] += dpost1
            stat_ref[2:3, 0:HD] += don_acc[0]
            stat_ref[2:3, HD:2 * HD] += don_acc[1]

    lat = lambda i: (jnp.maximum(i - n_ctx_tiles, 0), 0)
    full = lambda i: (i, 0)
    fixed = lambda i: (0, 0)

    def pcol(blk):
        return pl.BlockSpec((TM, HW), lambda i: (i, blk))

    in_specs = ([pl.BlockSpec((TM, d), lat)] + [pl.BlockSpec((TM, HW), full)] * 4
                + [pcol(C_HGATE), pcol(C_GGATE), pcol(9), pcol(10), pcol(11), pcol(12)]
                + [pl.BlockSpec((TM, d), lat), pl.BlockSpec((TM, d), lat)]
                + [pl.BlockSpec((8, d), fixed)] * 3 + [VMEM_SPEC] * 3)
    f = lambda w: jax.ShapeDtypeStruct((total, w), BF16)
    out_shape = [f(HW), f(HW), f(HW), f(HW), f(d), f(d), jax.ShapeDtypeStruct((rows, d), BF16),
                 jax.ShapeDtypeStruct((rows, d), BF16), jax.ShapeDtypeStruct((rows, d), BF16),
                 jax.ShapeDtypeStruct((8, d), F32)]
    out_specs = ([pl.BlockSpec((TM, HW), full)] * 4 + [pl.BlockSpec((TM, d), full)] * 2
                 + [pl.BlockSpec((TM, d), lat)] * 3 + [pl.BlockSpec((8, d), fixed)])
    return pl.pallas_call(
        body, name=name, grid=(n_tiles,), out_shape=out_shape, in_specs=in_specs, out_specs=out_specs,
        compiler_params=_cparams(dimension_semantics=("arbitrary",)),
    )(x_lat, *o_list, p, p, p, p, p, p, dz2, y1, modx, norms, onorms, w_br_hg, w_br_gla, w_out)


def _in_projection_bwd(ctx0, x0, dz2, modc, modx, pre1, w_t, pieces, n_ctx_tiles, name):
    d = x0.shape[1]
    rows = ctx0.shape[0] + x0.shape[0]
    lat_rows = dz2.shape[0]
    width = P_WIDTH
    n_pieces = len(pieces)

    def body(*refs):
        ctx_ref, x_ref, dz2_ref, modc_ref, modx_ref, pre_ref, w_ref = refs[:7]
        (dhq_f, dhq_b, dhi_f, dhi_b, dhf_f, dhf_b, dhgate, dgq_f, dgq_b, dgk_f, dgk_b, dgv_f, dgv_b, dggate,
         dghg, dggla, dlr_f, dlr_b) = refs[7:7 + n_pieces]
        dp_ref, gx_ref, stat_ref = refs[7 + n_pieces:]
        i = pl.program_id(0)
        is_ctx = i < n_ctx_tiles
        z = jnp.where(is_ctx, ctx_ref[...], x_ref[...])
        sections = [
            (0, dhq_f[...] + dhq_b[...]), (HW, dhi_f[...] + dhi_b[...]), (2 * HW, dhf_f[...]), (3 * HW, dhf_b[...]),
            (4 * HW, dhgate[...]), (5 * HW, dgq_f[...] + dgq_b[...]), (6 * HW, dgk_f[...] + dgk_b[...]),
            (7 * HW, dgv_f[...] + dgv_b[...]), (8 * HW, dggate[...]),
            (9 * HW, dghg[:, 0:HW]), (10 * HW, dghg[:, HW:2 * HW]),
            (11 * HW, dggla[:, 0:HW]), (12 * HW, dggla[:, HW:2 * HW]), (OFF_LR, dlr_f[...] + dlr_b[...])]
        dh = jnp.zeros((TM, d), F32)
        for off, val in sections:
            w = val.shape[1]
            vb = val.astype(BF16)
            dp_ref[off:off + w, :] = vb.T
            dh = dh + jnp.dot(vb, w_ref[_w_in_row(off):_w_in_row(off) + w, :], preferred_element_type=F32)
        n, r = _rms(z)
        pre = pre_ref[...]
        scale = jnp.where(is_ctx, modc_ref[1:2, :], modx_ref[1:2, :])
        nw = n * pre
        dshift = _colsum(dh)
        dscale = _colsum(dh * nw)
        dnw = dh * (1.0 + scale)
        dpre = _colsum(dnw * n)
        gx_ref[...] = dz2_ref[...] + _rms_bwd(dnw * pre, n, r)
        zero = jnp.zeros((1, d), F32)

        @pl.when(i == 0)
        def _():
            stat_ref[...] = jnp.zeros_like(stat_ref)

        stat_ref[0:1, :] += jnp.where(is_ctx, zero, dshift)
        stat_ref[1:2, :] += jnp.where(is_ctx, zero, dscale)
        stat_ref[2:3, :] += jnp.where(is_ctx, dshift, zero)
        stat_ref[3:4, :] += jnp.where(is_ctx, dscale, zero)
        stat_ref[4:5, :] += dpre

    full = lambda i: (i, 0)
    lat = lambda i: (jnp.maximum(i - n_ctx_tiles, 0), 0)
    fixed = lambda i: (0, 0)
    piece_specs = [pl.BlockSpec((TM, a.shape[1]), full) for a in pieces]
    in_specs = [pl.BlockSpec((TM, d), lambda i: (jnp.minimum(i, n_ctx_tiles - 1), 0)), pl.BlockSpec((TM, d), lat),
                pl.BlockSpec((TM, d), lat), pl.BlockSpec((8, d), fixed),
                pl.BlockSpec((8, d), fixed), pl.BlockSpec((1, d), fixed), VMEM_SPEC] + piece_specs
    return pl.pallas_call(
        body, name=name, grid=(rows // TM,),
        out_shape=[jax.ShapeDtypeStruct((width, rows), BF16), jax.ShapeDtypeStruct((lat_rows, d), F32),
                   jax.ShapeDtypeStruct((8, d), F32)],
        in_specs=in_specs,
        out_specs=[pl.BlockSpec((width, TM), lambda i: (0, i)), pl.BlockSpec((TM, d), lat),
                   pl.BlockSpec((8, d), fixed)],
        compiler_params=_cparams(dimension_semantics=("arbitrary",)),
    )(ctx0, x0, dz2, modc, modx, pre1, w_t, *pieces)


def _transposed_lhs_matmul(x_ref, dy_ref, o_ref, xt_ref):
    @pl.when(pl.program_id(1) == 0)
    def _():
        xt_ref[...] = x_ref[...].T

    o_ref[...] = jnp.dot(xt_ref[...], dy_ref[...], preferred_element_type=F32)


def _w_in_grad(dp_t, h1, n_cols, name, after=()):
    rows, d = h1.shape
    n_main = OFF_LR // HW
    lr0 = _w_in_row(OFF_LR)

    def body(x_ref, xlr_ref, h_ref, *rest):
        o_hbm, acc_ref, sems = rest[len(after):]
        i = pl.program_id(0)
        slot = i % 2

        def main_copy(step):
            row = jnp.where(step < 9, step * HW, step * HW + 2 * RANK)
            return pltpu.make_async_copy(acc_ref.at[step % 2], o_hbm.at[pl.ds(pl.multiple_of(row, 8), HW), :],
                                         sems.at[step % 2])

        @pl.when(i > 1)
        def _():
            main_copy(i - 2).wait()

        @pl.when(i < n_main)
        def _():
            acc_ref[slot] = jnp.dot(x_ref[...], h_ref[...], preferred_element_type=F32)
            main_copy(i).start()

        @pl.when(i == n_main)
        def _():
            acc_ref[slot, 0:128, :] = jnp.dot(xlr_ref[...], h_ref[...], preferred_element_type=F32)
            lr_copy = pltpu.make_async_copy(acc_ref.at[slot, 0:2 * RANK, :], o_hbm.at[lr0:lr0 + 2 * RANK, :],
                                            sems.at[slot])
            lr_copy.start()
            main_copy(i - 1).wait()
            lr_copy.wait()

    return pl.pallas_call(
        body, name=name, grid=(n_main + 1,),
        out_shape=jax.ShapeDtypeStruct((n_cols, d), F32),
        in_specs=[pl.BlockSpec((HW, rows), lambda i: (jnp.minimum(i, n_main - 1), 0)),
                  pl.BlockSpec((128, rows), lambda i: (OFF_LR // 128, 0)),
                  pl.BlockSpec((rows, d), lambda i: (0, 0))] + [ANY_SPEC] * len(after),
        out_specs=ANY_SPEC,
        scratch_shapes=[pltpu.VMEM((2, HW, d), F32), pltpu.SemaphoreType.DMA((2,))],
        compiler_params=_cparams(dimension_semantics=("arbitrary",)),
    )(dp_t, dp_t, h1, *after)


def _weight_grad(xs, dy, name, tk=None, tn=512, k_first=0, k_tiles=None):
    rows = dy.shape[0]
    n = dy.shape[1]
    tn_ = min(tn, n)
    tk_ = xs.shape[1] if tk is None else tk
    k_tiles = xs.shape[1] // tk_ if k_tiles is None else k_tiles
    k = k_tiles * tk_

    return pl.pallas_call(
        functools.partial(_transposed_lhs_matmul), name=name, grid=(k_tiles, n // tn_),
        out_shape=jax.ShapeDtypeStruct((k, n), F32),
        in_specs=[pl.BlockSpec((rows, tk_), lambda i, j: (0, i + k_first)),
                  pl.BlockSpec((rows, tn_), lambda i, j: (0, j))],
        out_specs=pl.BlockSpec((tk_, tn_), lambda i, j: (i, j)),
        scratch_shapes=[pltpu.VMEM((tk_, rows), BF16)],
        compiler_params=_cparams(dimension_semantics=("parallel", "arbitrary")),
    )(xs, dy)


def _running_sums(xs, fws):
    c = xs[0].shape[0]
    row = lax.broadcasted_iota(jnp.int32, (c, 1), 0)
    s = 1
    while s < c:
        xs = [x + (jnp.where(row >= s, pltpu.roll(x, s, axis=0), 0.0) if fw else
                   jnp.where(row < c - s, pltpu.roll(x, c - s, axis=0), 0.0)) for x, fw in zip(xs, fws)]
        s *= 2
    return xs


def _chunks_terms(qs, ks, gs, fws):
    c = CHUNK
    n = len(qs)
    r = lax.broadcasted_iota(jnp.int32, (c, c), 0)
    s = lax.broadcasted_iota(jnp.int32, (c, c), 1)
    row = lax.broadcasted_iota(jnp.int32, (c, 1), 0)
    per_dir = {}
    for fw in set(fws):
        pos = row if fw else (c - 1 - row)
        per_dir[fw] = dict(
            causal=(s <= r) if fw else (s >= r), causal_t=(s >= r) if fw else (s <= r), pos=pos,
            in_blk=[(pos >= SUB * j) & (pos < SUB * (j + 1)) for j in range(NSUB)],
            start_row=[None] + [SUB * j - 1 if fw else c - SUB * j for j in range(1, NSUB)],
            rend=c - 1 if fw else 0)
    dirs = [per_dir[fw] for fw in fws]
    cums = _running_sums(gs, fws)
    starts = [[None] + [cum[d["start_row"][j]:d["start_row"][j] + 1, :] for j in range(1, NSUB)]
              for cum, d in zip(cums, dirs)]
    es = [[jnp.exp(cum) for cum in cums]]
    for j in range(1, NSUB):
        es.append([jnp.exp(jnp.where(d["pos"] >= SUB * j, cum - st[j], -1e30)) for cum, st, d in zip(cums, starts, dirs)])
    owns = [functools.reduce(lambda rest, j: jnp.where(d["in_blk"][j], st[j], rest), range(1, NSUB), 0.0)
            for st, d in zip(starts, dirs)]
    kscales = [jnp.exp(own - cum) for own, cum in zip(owns, cums)]
    cends = [cum[d["rend"]:d["rend"] + 1, :] for cum, d in zip(cums, dirs)]
    tails = [jnp.exp(cend - cum) for cend, cum in zip(cends, cums)]
    qcats = [jnp.concatenate([q * es[j][i] for j in range(NSUB)], axis=1).astype(BF16) for i, q in enumerate(qs)]
    kts = [k * ksc for k, ksc in zip(ks, kscales)]
    kms = [jnp.concatenate([jnp.where(d["in_blk"][j], kt, 0.0) for j in range(NSUB)], axis=1).astype(BF16)
           for kt, d in zip(kts, dirs)]
    e_by_lane = [[es[j][i] for j in range(NSUB)] for i in range(n)]
    return dict(dirs=dirs, e=e_by_lane, kscale=kscales, cend=cends, tail=tails, qcat=qcats, km=kms, kt=kts)


def _chunks_fwd(qs, ks, vs, gs, st0s, fws):
    t = _chunks_terms(qs, ks, gs, fws)
    scores = [_dot_nt(qc, km) for qc, km in zip(t["qcat"], t["km"])]
    a = [jnp.where(d["causal"], sc, 0.0) for sc, d in zip(scores, t["dirs"])]
    inter = [_dot_nt(qc[:, 0:HD], st0) for qc, st0 in zip(t["qcat"], st0s)]
    intra = [_dot(a_, v) for a_, v in zip(a, vs)]
    os_ = [x + y for x, y in zip(intra, inter)]
    upd = [_dot_tn(v, k * tl) for v, k, tl in zip(vs, ks, t["tail"])]
    st1s = [st0 * jnp.exp(ce) + u for st0, ce, u in zip(st0s, t["cend"], upd)]
    return os_, st1s


def _chunks_bwd(qs, ks, vs, gs, st0s, dos, dst1s, fws):
    n = len(qs)
    t = _chunks_terms(qs, ks, gs, fws)
    qcat, km, e, dirs = t["qcat"], t["km"], t["e"], t["dirs"]
    a_t = [jnp.where(d["causal_t"], _dot_nt(km_, qc), 0.0) for km_, qc, d in zip(km, qcat, dirs)]
    ktail = [k * tl for k, tl in zip(ks, t["tail"])]
    dv_a = [_dot(at, do) for at, do in zip(a_t, dos)]
    dv_b = [_dot_nt(kt, ds) for kt, ds in zip(ktail, dst1s)]
    dv = [x + y for x, y in zip(dv_a, dv_b)]
    da = [jnp.where(d["causal"], _dot_nt(do, v), 0.0) for do, v, d in zip(dos, vs, dirs)]
    da_t = [jnp.where(d["causal_t"], _dot_nt(v, do), 0.0) for do, v, d in zip(dos, vs, dirs)]
    dqcat = [_dot(da_, km_) for da_, km_ in zip(da, km)]
    dq_inter = [e[i][0] * _dot(dos[i], st0s[i]) for i in range(n)]
    dkm = [_dot(dat, qc) for dat, qc in zip(da_t, qcat)]
    dk_inter = [_dot(v, ds) * tl for v, ds, tl in zip(vs, dst1s, t["tail"])]
    dq = [dq_inter[i] + sum(e[i][j] * dqcat[i][:, j * HD:(j + 1) * HD] for j in range(NSUB)) for i in range(n)]
    dkt = [sum(jnp.where(dirs[i]["in_blk"][j], dkm[i][:, j * HD:(j + 1) * HD], 0.0) for j in range(NSUB))
           for i in range(n)]
    dk = [dkt[i] * t["kscale"][i] + dk_inter[i] for i in range(n)]
    dcum = [qs[i] * dq_inter[i] - ks[i] * dk_inter[i] - t["kt"][i].astype(BF16).astype(F32) * dkt[i]
            + sum(qcat[i][:, j * HD:(j + 1) * HD].astype(F32) * dqcat[i][:, j * HD:(j + 1) * HD] for j in range(NSUB))
            for i in range(n)]
    ecend = [jnp.exp(ce) for ce in t["cend"]]
    end = [ecend[i] * _colsum(st0s[i] * dst1s[i]) + _colsum(ks[i] * dk_inter[i]) for i in range(n)]
    sums = _running_sums(dcum, [not fw for fw in fws])
    dg = [sm + en for sm, en in zip(sums, end)]
    upd = [_dot_tn(dos[i], qs[i] * e[i][0]) for i in range(n)]
    dst0 = [dst1s[i] * ecend[i] + upd[i] for i in range(n)]
    return dq, dk, dv, dg, dst0


def _chunk_index(step, n_ctx_chunks, n_chunks, fw):
    if fw:
        return step
    return jnp.where(step < n_ctx_chunks, n_ctx_chunks - 1 - step, n_chunks - 1 + n_ctx_chunks - step)


def _hg_inputs(hq, hf, lbv, d_idx, sl):
    lb = _sigmoid(lbv[d_idx:d_idx + 1, sl] - lbv[2 + d_idx:3 + d_idx, sl])
    sg = _sigmoid(hf)
    f = lb + (1.0 - lb) * sg
    return _silu(hq), 1.0 - f, jnp.log(f), f, sg, lb


def _scan_fwd_both(p, branch_sides, n_ctx_chunks, name):
    rows = p.shape[0]
    n_chunks = rows // CHUNK
    n_ins = [4 if branch == "hg" else 6 for branch, _ in branch_sides]
    n_in_all = 2 * sum(n_ins)
    n_br = len(branch_sides)

    def body(*refs):
        ins, outs, state = refs[:n_in_all], refs[n_in_all:n_in_all + 4 * n_br], refs[-1]

        @pl.when(pl.program_id(0) == 0)
        def _():
            state[...] = jnp.zeros_like(state)

        lanes, where = [], []
        pos = 0
        for bi, (branch, _) in enumerate(branch_sides):
            hg = branch == "hg"
            n_in = n_ins[bi]
            for di, fw in enumerate((True, False)):
                r = ins[pos:pos + n_in]
                pos += n_in
                o_ref, st_ref = outs[4 * bi + 2 * di], outs[4 * bi + 2 * di + 1]
                if hg:
                    a_ref, b_ref, c_ref, lb_ref = r
                else:
                    a_ref, b_ref, c_ref, lr_ref, wgk_ref, bgk_ref = r
                    logits = _dot(lr_ref[...], wgk_ref[...]) + bgk_ref[...]
                    g_all = _log_sigmoid(logits) * (1.0 / GATE_NORM)
                for h in range(NH):
                    sl = slice(h * HD, (h + 1) * HD)
                    if hg:
                        q, k, g, _, _, _ = _hg_inputs(a_ref[:, sl], c_ref[:, sl], lb_ref[...], di, sl)
                        v = b_ref[:, sl]
                    else:
                        q, k, v, g = a_ref[:, sl] * (HD ** -0.5), b_ref[:, sl], c_ref[:, sl], g_all[:, sl]
                    lanes.append((q, k, v, g, state[2 * bi + di, h], fw))
                    where.append((2 * bi + di, h, sl, o_ref, st_ref))
        qs, ks, vs, gs, st0s, fws = (list(col) for col in zip(*lanes))
        os_, st1s = _chunks_fwd(qs, ks, vs, gs, st0s, fws)
        for (si, h, sl, o_ref, st_ref), st0, o, st1 in zip(where, st0s, os_, st1s):
            st_ref[0, h] = st0
            o_ref[:, sl] = o
            state[si, h] = st1

    fixed = lambda j: (0, 0)
    in_specs, args, out_specs = [], [], []
    for branch, side in branch_sides:
        for di, fw in enumerate((True, False)):
            chunk = functools.partial(_chunk_index, n_ctx_chunks=n_ctx_chunks, n_chunks=n_chunks, fw=fw)

            def cmap(blk, width=HW, chunk=chunk):
                return pl.BlockSpec((CHUNK, width), lambda j: (chunk(j), blk))

            if branch == "hg":
                in_specs += [cmap(C_HQ), cmap(C_HI), cmap(C_HF_FW + di), pl.BlockSpec((4, HW), fixed)]
                args += [p, p, p, side]
            else:
                in_specs += [cmap(C_GQ), cmap(C_GK), cmap(C_GV), cmap(OFF_LR // 128, 128),
                             pl.BlockSpec((128, HW), fixed), pl.BlockSpec((1, HW), fixed)]
                args += [p, p, p, p, side[di][0], side[di][1]]
            out_specs += [cmap(0), pl.BlockSpec((1, NH, HD, HD), lambda j, chunk=chunk: (chunk(j), 0, 0, 0))]
    return pl.pallas_call(
        body, name=name, grid=(n_chunks,),
        out_shape=[jax.ShapeDtypeStruct((rows, HW), F32),
                   jax.ShapeDtypeStruct((n_chunks, NH, HD, HD), F32)] * (2 * n_br),
        in_specs=in_specs, out_specs=out_specs,
        scratch_shapes=[pltpu.VMEM((2 * n_br, NH, HD, HD), F32)],
        compiler_params=_cparams(dimension_semantics=("arbitrary",)),
    )(*args)


def _scan_bwd_both(p, branch_items, n_ctx_chunks, name):
    rows = p.shape[0]
    n_chunks = rows // CHUNK
    n_ins = [6 if item[0] == "hg" else 8 for item in branch_items]
    n_outs = [4 if item[0] == "hg" else 6 for item in branch_items]
    n_in_all, n_out_all = 2 * sum(n_ins), 2 * sum(n_outs)

    def body(*refs):
        ins, outs, dstate = refs[:n_in_all], refs[n_in_all:n_in_all + n_out_all], refs[-1]
        first = pl.program_id(0) == 0

        @pl.when(first)
        def _():
            dstate[...] = jnp.zeros_like(dstate)

        lanes, where, extra, ctx = [], [], [], []
        ipos = opos = 0
        for bi, item in enumerate(branch_items):
            hg = item[0] == "hg"
            for di, fw in enumerate((True, False)):
                r, w = ins[ipos:ipos + n_ins[bi]], outs[opos:opos + n_outs[bi]]
                ipos += n_ins[bi]
                opos += n_outs[bi]
                if hg:
                    a_ref, b_ref, c_ref, lb_ref, st_ref, do_ref = r
                    acc_refs = (w[3],)
                else:
                    a_ref, b_ref, c_ref, lr_ref, wgk_ref, bgk_ref, st_ref, do_ref = r
                    acc_refs = (w[4], w[5])
                    lr = lr_ref[...]
                    logits = _dot(lr, wgk_ref[...]) + bgk_ref[...]
                    g_all = _log_sigmoid(logits) * (1.0 / GATE_NORM)

                @pl.when(first)
                def _(acc_refs=acc_refs):
                    for ref in acc_refs:
                        ref[...] = jnp.zeros_like(ref)

                for h in range(NH):
                    sl = slice(h * HD, (h + 1) * HD)
                    if hg:
                        hq, hf = a_ref[:, sl], c_ref[:, sl]
                        q, k, g, f, sg, lb = _hg_inputs(hq, hf, lb_ref[...], di, sl)
                        v = b_ref[:, sl]
                        extra.append((hq, f, sg, lb))
                    else:
                        q, k, v, g = a_ref[:, sl] * (HD ** -0.5), b_ref[:, sl], c_ref[:, sl], g_all[:, sl]
                        extra.append(None)
                    lanes.append((q, k, v, g, st_ref[0, h], do_ref[:, sl], dstate[2 * bi + di, h], fw))
                    where.append((2 * bi + di, h, sl))
                ctx.append((hg, w, None if hg else (lr, logits, wgk_ref)))

        dqs, dks, dvs, dgs, dst0s = [], [], [], [], []
        for lo in range(0, len(lanes), BWD_LANES):
            cols = [list(col) for col in zip(*lanes[lo:lo + BWD_LANES])]
            for acc, part in zip((dqs, dks, dvs, dgs, dst0s), _chunks_bwd(*cols)):
                acc.extend(part)
        dg_parts = [[] for _ in ctx]
        for (si, h, sl), ex, dq, dk, dv, dg, dst0 in zip(where, extra, dqs, dks, dvs, dgs, dst0s):
            dstate[si, h] = dst0
            hg, w, _ = ctx[si]
            if hg:
                hq, f, sg, lb = ex
                da_ref, db_ref, dc_ref, dlb_ref = w
                da_ref[:, sl] = (dq * _dsilu(hq)).astype(BF16)
                db_ref[:, sl] = dv.astype(BF16)
                df = dg / f - dk
                dc_ref[:, sl] = (df * (1.0 - lb) * sg * (1.0 - sg)).astype(BF16)
                dlb_ref[0:1, sl] += _colsum(df * (1.0 - sg))
            else:
                da_ref, db_ref, dc_ref = w[:3]
                da_ref[:, sl] = (dq * (HD ** -0.5)).astype(BF16)
                db_ref[:, sl] = dk.astype(BF16)
                dc_ref[:, sl] = dv.astype(BF16)
                dg_parts[si].append(dg)
        for si, (hg, w, more) in enumerate(ctx):
            if not hg:
                dlr_ref, dwgk_ref, dbias_ref = w[3:]
                lr, logits, wgk_ref = more
                dlogits = jnp.concatenate(dg_parts[si], axis=1) * (1.0 / GATE_NORM) * (1.0 - _sigmoid(logits))
                dlr_ref[...] = _dot_nt(dlogits, wgk_ref[...]).astype(BF16)
                dwgk_ref[...] += _dot_tn(lr, dlogits)
                dbias_ref[0:1, :] += _colsum(dlogits)

    fixed = lambda j: (0, 0)
    big = jax.ShapeDtypeStruct((rows, HW), BF16)
    in_specs, args, out_shape, out_specs = [], [], [], []
    for branch, side, states, d_o in branch_items:
        for di, fw in enumerate((True, False)):
            def chunk_of(j, fw=fw):
                return _chunk_index(n_chunks - 1 - j, n_ctx_chunks, n_chunks, fw)

            def cmap(blk, width=HW, chunk_of=chunk_of):
                return pl.BlockSpec((CHUNK, width), lambda j: (chunk_of(j), blk))

            st_spec = pl.BlockSpec((1, NH, HD, HD), lambda j, chunk_of=chunk_of: (chunk_of(j), 0, 0, 0))
            if branch == "hg":
                in_specs += [cmap(C_HQ), cmap(C_HI), cmap(C_HF_FW + di), pl.BlockSpec((4, HW), fixed), st_spec,
                             cmap(0)]
                args += [p, p, p, side, states[di], d_o]
                out_shape += [big, big, big, jax.ShapeDtypeStruct((8, HW), F32)]
                out_specs += [cmap(0), cmap(0), cmap(0), pl.BlockSpec((8, HW), fixed)]
            else:
                in_specs += [cmap(C_GQ), cmap(C_GK), cmap(C_GV), cmap(OFF_LR // 128, 128),
                             pl.BlockSpec((128, HW), fixed), pl.BlockSpec((1, HW), fixed), st_spec, cmap(0)]
                args += [p, p, p, p, side[di][0], side[di][1], states[di], d_o]
                out_shape += [big, big, big, jax.ShapeDtypeStruct((rows, 128), BF16),
                              jax.ShapeDtypeStruct((128, HW), F32), jax.ShapeDtypeStruct((8, HW), F32)]
                out_specs += [cmap(0), cmap(0), cmap(0), cmap(0, 128), pl.BlockSpec((128, HW), fixed),
                              pl.BlockSpec((8, HW), fixed)]
    return pl.pallas_call(
        body, name=name, grid=(n_chunks,), out_shape=out_shape, in_specs=in_specs, out_specs=out_specs,
        scratch_shapes=[pltpu.VMEM((2 * len(branch_items), NH, HD, HD), F32)],
        compiler_params=_cparams(dimension_semantics=("arbitrary",)),
    )(*args)


SMALL_ROWS = 56
ROWS_MOD_X = (0, 1, 8, 16, 17, 18)
ROWS_MOD_C = (2, 3)
ROW_PRE1, ROW_POST1, ROW_ONORM, ROW_PRE2, ROW_POST2, ROW_LB, ROW_BGK, ROW_WGK = 4, 9, 10, 19, 20, 24, 32, 40
ROW_LOSS = 21


def _reduce_small(gathered, lb_full, name, after=()):
    _, _, d = gathered.shape

    def body(g_ref, lb_ref, *rest):
        sum_ref, dmod_ref, dbmod_ref, dlb_ref = rest[len(after):]
        total = g_ref[0]
        for b in range(1, N_DEV):
            total = total + g_ref[b]
        sum_ref[...] = total
        dmod_ref[...] = jnp.zeros_like(dmod_ref)
        for m in range(N_MOD):
            col = slice(m * d, (m + 1) * d)
            acc = jnp.zeros((1, d), F32)
            for b in range(N_DEV):
                row = g_ref[b, ROWS_MOD_X[m]:ROWS_MOD_X[m] + 1, :]
                dmod_ref[b:b + 1, col] = row
                acc = acc + row
            if m < 2:
                ctx_row = total[ROWS_MOD_C[m]:ROWS_MOD_C[m] + 1, :]
                dmod_ref[8:9, col] = ctx_row
                acc = acc + ctx_row
            dbmod_ref[:, col] = acc
        lbv = lb_ref[...]
        for dd in range(2):
            lb = _sigmoid(lbv[dd:dd + 1, :] - lbv[2 + dd:3 + dd, :])
            gl = total[ROW_LB:ROW_LB + 1, dd * HW:(dd + 1) * HW] * lb * (1.0 - lb)
            dlb_ref[dd:dd + 1, :] = gl
            dlb_ref[2 + dd:3 + dd, :] = -gl

    return pl.pallas_call(
        body, name=name,
        out_shape=[jax.ShapeDtypeStruct((SMALL_ROWS, d), F32), jax.ShapeDtypeStruct((16, N_MOD * d), F32),
                   jax.ShapeDtypeStruct((1, N_MOD * d), F32), jax.ShapeDtypeStruct((4, HW), F32)],
        in_specs=[VMEM_SPEC] * 2 + [ANY_SPEC] * len(after), out_specs=[VMEM_SPEC] * 4, compiler_params=_cparams(),
    )(gathered, lb_full, *after)


def _c_ctx_grad(gathered, c_ctx_row, name):
    def body(g_ref, c_ref, o_ref):
        acc = g_ref[0, 0:1, :]
        for chip in range(1, N_CHIP):
            acc = acc + g_ref[2 * chip, 0:1, :]
        o_ref[...] = acc * _dsilu(c_ref[...])

    return pl.pallas_call(
        body, name=name, out_shape=jax.ShapeDtypeStruct(c_ctx_row.shape, F32),
        in_specs=[VMEM_SPEC] * 2, out_specs=VMEM_SPEC, compiler_params=_cparams(),
    )(gathered, c_ctx_row)


def _blocked(full, n_blocks):
    k, n = full.shape
    return full.reshape(k, n_blocks, n // n_blocks).transpose(1, 0, 2)


def _unblocked(blocks):
    nb, k, n = blocks.shape
    return blocks.transpose(1, 0, 2).reshape(k, nb * n)


def _sample_front(x0, ctx0, modc, modx, norm_pre1, lb_full, gla_side, w_in_r):
    ctx_len = ctx0.shape[0]
    n_ctx_tiles = ctx_len // TM
    n_ctx_chunks = ctx_len // CHUNK
    h1, p = _in_projection(ctx0, x0, modc, modx, norm_pre1, w_in_r, n_ctx_tiles, "in_projection")
    (o_hg_fw, st_hg_fw, o_hg_bw, st_hg_bw, o_gla_fw, st_gla_fw, o_gla_bw, st_gla_bw) = _scan_fwd_both(
        p, [("hg", lb_full), ("gla", gla_side)], n_ctx_chunks, "scan_fwd")
    return dict(h1=h1, p=p, o_list=[o_hg_fw, o_hg_bw, o_gla_fw, o_gla_bw],
                states=[st_hg_fw, st_hg_bw, st_gla_fw, st_gla_bw])


def _sample_back(reduce, front, x0, ctx0, target0, modc, modx, norm_pre1, norms, onorms, lb_full, gla_side, w_in_r,
                 wbh, wbg, wout, ffn_weights):
    seq, d = x0.shape
    ctx_len = ctx0.shape[0]
    n_ctx_tiles = ctx_len // TM
    n_tiles = (ctx_len + seq) // TM
    n_ctx_chunks = ctx_len // CHUNK
    h1, p, o_list = front["h1"], front["p"], front["o_list"]
    st_hg_fw, st_hg_bw, st_gla_fw, st_gla_bw = front["states"]
    z2, y1, merged, og_hg, og_gla = _mixer_tail_fwd(x0, p, o_list, modx, norms, onorms, wbh, wbg, wout, n_ctx_tiles,
                                                    "mixer_tail")
    wg, wu, wd = ffn_weights([z2])
    loss_part, dz2, h2, a_act, du, dv, dy2, stat_ffn = _ffn_fwd_bwd(z2, modx, norms, wg, wu, wd, target0, "ffn")
    dff = wg.shape[0]
    tok = reduce("ffn", [_weight_grad(du, h2, "grad_w_ff_gate", tk=dff // 2, tn=d),
                         _weight_grad(dv, h2, "grad_w_ff_up", tk=dff // 2, tn=d),
                         _weight_grad(a_act, dy2, "grad_w_ff_down", tk=dff // 2)])

    (d_ohg, d_ogla, d_hgate, d_ggate, d_ghg, d_ggla, dy1, db_hg, db_gla, stat_mix) = _mixer_tail_bwd(
        x0, p, o_list, dz2, y1, modx + tok, norms, onorms, wbh, wbg, wout, n_ctx_tiles, n_tiles, "mixer_tail_bwd")
    tok = reduce("mix", [_weight_grad(og_hg, db_hg, "grad_w_br_hg"), _weight_grad(og_gla, db_gla, "grad_w_br_gla"),
                         _weight_grad(merged, dy1, "grad_w_out")])
    tok = tok + reduce("push_ffn", [dy1])
    gla_b = [(wgk, bias + tok) for wgk, bias in gla_side]
    (dgq_f, dgk_f, dgv_f, dlr_f, dwgk_f, dbgk_f, dgq_b, dgk_b, dgv_b, dlr_b, dwgk_b, dbgk_b,
     dhq_f, dhi_f, dhf_f, dlb_f, dhq_b, dhi_b, dhf_b, dlb_b) = _scan_bwd_both(
        p, [("gla", gla_b, (st_gla_fw, st_gla_bw), d_ogla), ("hg", lb_full, (st_hg_fw, st_hg_bw), d_ohg)],
        n_ctx_chunks, "scan_bwd")
    tok = reduce("push_mix", [dbgk_f])
    pieces = [dhq_f, dhq_b, dhi_f, dhi_b, dhf_f, dhf_b, d_hgate, dgq_f, dgq_b, dgk_f, dgk_b, dgv_f, dgv_b, d_ggate,
              d_ghg, d_ggla, dlr_f, dlr_b]
    dp, grad_x, stat_in = _in_projection_bwd(ctx0, x0, dz2, modc, modx, norm_pre1 + tok, w_in_r, pieces, n_ctx_tiles,
                                             "in_projection_bwd")

    started = reduce("small_start", dict(stat_in=stat_in, stat_mix=stat_mix, stat_ffn=stat_ffn, dlb=(dlb_f, dlb_b),
                                         dwgk=(dwgk_f, dwgk_b), dbgk=(dbgk_f, dbgk_b)))
    reduce("in", [_w_in_grad(dp, h1, w_in_r.shape[0], "grad_w_in", after=[started])])
    reduce("small", None)
    reduce("push_in", [])
    return dict(loss_part=loss_part, grad_x=grad_x)


def kernel(x, c, ctx, c_ctx, w_mod, b_mod, norm_pre1, norm_post1, norm_pre2, norm_post2, w_in, hg_lb, hg_onorm, gla_w_gk, gla_b_gk, gla_onorm, w_br_hg, w_br_gla, w_out, w_ff_gate, w_ff_up, w_ff_down, loss_target, m_c_ctx, m_w_mod, m_b_mod, m_norm_pre1, m_norm_post1, m_norm_pre2, m_norm_post2, m_w_in, m_hg_lb, m_hg_onorm, m_gla_w_gk, m_gla_b_gk, m_gla_onorm, m_w_br_hg, m_w_br_gla, m_w_out, m_w_ff_gate, m_w_ff_up, m_w_ff_down, v_c_ctx, v_w_mod, v_b_mod, v_norm_pre1, v_norm_post1, v_norm_pre2, v_norm_post2, v_w_in, v_hg_lb, v_hg_onorm, v_gla_w_gk, v_gla_b_gk, v_gla_onorm, v_w_br_hg, v_w_br_gla, v_w_out, v_w_ff_gate, v_w_ff_up, v_w_ff_down):
    seq, d = x.shape[1], x.shape[2]
    ctx_len = ctx.shape[1]
    assert seq % TM == 0 and ctx_len % TM == 0 and d == 2 * HW
    ax, ay, ac = lax.axis_index("x"), lax.axis_index("y"), lax.axis_index("c")
    chip = 2 * ax + ay
    dev = 2 * chip + ac
    c_arr = jnp.reshape(ac, (1,)).astype(jnp.int32)
    chip_arr = jnp.reshape(chip, (1,)).astype(jnp.int32)
    transposed = ("w_in", "w_ff_gate", "w_ff_up")
    view = lambda a, nm: a[0].T if nm in transposed else a[0]

    sems_in, lands_in, token_in0 = _blocks_start([_cast_into_blocks(chip_arr, view(w_in, "w_in"), "cast_w_in")],
                                                 "gather_w_in_start")

    nc = d // 128
    pad8 = lambda a: jnp.pad(a, ((0, -a.shape[0] % 8), (0, 0)))
    small1 = jnp.concatenate([c.reshape(nc, 128) + token_in0[0, 0], pad8(hg_lb.reshape(4, 128)),
                              gla_w_gk.reshape(2 * RANK, 128), pad8(gla_b_gk.reshape(2, 128))], axis=0)
    blocks = [_cast_into_blocks(chip_arr, view(w_, nm), "cast_" + nm) for w_, nm in (
        (w_br_hg, "w_br_hg"), (w_br_gla, "w_br_gla"), (w_out, "w_out"), (w_ff_gate, "w_ff_gate"),
        (w_ff_up, "w_ff_up"), (w_ff_down, "w_ff_down"))]
    got1 = _allgather8(small1, "gather_small_params", after=blocks)
    c_all = got1[:, :nc, :].reshape(N_DEV, d)
    per_chip = got1[0::2]
    lb_full = per_chip[:, nc:nc + 4, :].transpose(1, 0, 2).reshape(4, HW)
    wgk_full = per_chip[:, nc + 8:nc + 8 + 2 * RANK, :].transpose(1, 0, 2).reshape(2, RANK, HW)
    bgk_full = per_chip[:, nc + 8 + 2 * RANK:nc + 10 + 2 * RANK, :].transpose(1, 0, 2).reshape(2, HW)
    wgk_pad = [jnp.zeros((128, HW), F32).at[dd * RANK:(dd + 1) * RANK].set(wgk_full[dd]) for dd in range(2)]
    bgk = [bgk_full[dd:dd + 1] for dd in range(2)]

    n_mod_cols = w_mod.shape[2]
    cond = jnp.concatenate([c_all, pad8(c_ctx.reshape(1, d))], axis=0)
    b_cols = lax.dynamic_slice(b_mod, (0, chip * n_mod_cols), (1, n_mod_cols))
    lands_in = _blocks_wait(sems_in, lands_in, [got1], "gather_w_in_wait")
    fwd_sems, lands_in, fwd_token = _forward_start(lands_in, "gather_w_in_forward_start")
    mod_part = _mod_forward(cond + fwd_token[0, 0], w_mod[0], b_cols, "mod_forward")
    mod_got = _allgather8(mod_part, "gather_mod")
    mod_all = mod_got[0::2].transpose(1, 0, 2).reshape(16, N_CHIP * n_mod_cols)
    modx = pad8(lax.dynamic_slice(mod_all, (dev, 0), (1, N_MOD * d)).reshape(N_MOD, d))
    modc = pad8(mod_all[8].reshape(N_MOD, d))

    gathered_in = _forward_wait(fwd_sems, lands_in, [mod_got], "gather_w_in_forward_wait")
    sems, lands, token = _blocks_start(blocks, "gather_rest_start", after=[gathered_in[0]])
    w_in_r = gathered_in[0].reshape(-1, d)

    norms = jnp.concatenate([norm_pre1, norm_post1, norm_pre2, norm_post2, jnp.zeros((4, d), F32)], axis=0)
    onorms = jnp.zeros((8, d), F32).at[0, :HD].set(hg_onorm[0]).at[1, :HD].set(gla_onorm[0])
    gla_side = [(wgk_pad[dd], bgk[dd]) for dd in range(2)]
    modx = modx + token[0, 0]
    front = _sample_front(x[0], ctx[0], modc, modx, norm_pre1, lb_full, gla_side, w_in_r)
    lands = _blocks_wait(sems, lands, front["o_list"], "gather_rest_wait")
    gathered = _blocks_finish(lands[:3], "gather_mix_finish")
    wbh, wbg = _unblocked(gathered[0]), _unblocked(gathered[1])
    wout = gathered[2].reshape(d, d)
    ffn_sems, ffn_lands, ffn_token = _forward_start(lands[3:], "gather_ffn_forward_start")
    onorms = onorms + ffn_token[0, 0]

    def ffn_weights(after):
        got = _forward_wait(ffn_sems, ffn_lands, after, "gather_ffn_forward_wait")
        return tuple(g.reshape(-1, d) for g in got)

    dff = w_ff_down.shape[1] * N_CHIP
    groups = {"ffn": ["w_ff_gate", "w_ff_up", "w_ff_down"], "mix": ["w_br_hg", "w_br_gla", "w_out"], "in": ["w_in"]}
    row_sharded = {"w_out": d // N_CHIP, "w_ff_down": dff // N_CHIP, "w_ff_gate": dff // N_CHIP,
                   "w_ff_up": dff // N_CHIP, "w_in": w_in.shape[2]}
    in_flight, to_sibling, small = {}, {}, {}

    def reduce_small_start(stats):
        small2 = jnp.concatenate([
            stats["stat_in"], stats["stat_mix"], stats["stat_ffn"],
            jnp.concatenate(stats["dlb"], axis=1), jnp.concatenate(stats["dbgk"], axis=1),
            jnp.concatenate([stats["dwgk"][0][0:RANK], stats["dwgk"][1][RANK:2 * RANK]], axis=1)], axis=0)
        assert small2.shape[0] == SMALL_ROWS
        sems_, land_, token_ = _allgather8_start(small2, "gather_small_grads_start")
        small.update(gathering=(sems_, land_))
        return token_

    def reduce_small():
        got2 = small["gathered"]
        total, dmod_all, g_b_mod, g_lb_full = _reduce_small(got2, lb_full, "reduce_small", to_sibling["in"][1])
        dmod_cols = lax.dynamic_slice(dmod_all, (0, chip * n_mod_cols), (16, n_mod_cols))
        g_w_mod, cctx_part = _mod_backward(cond, w_mod[0], dmod_cols, "mod_backward")
        got3 = _allgather8(cctx_part, "gather_c_ctx_grad")
        g_c_ctx = _c_ctx_grad(got3, c_ctx.reshape(1, d), "c_ctx_grad")
        small.update(total=total, g_b_mod=g_b_mod, g_lb_full=g_lb_full, g_w_mod=g_w_mod, g_c_ctx=g_c_ctx)

    def reduce(group, grads):
        if group == "small_start":
            return reduce_small_start(grads)
        if group == "small":
            return reduce_small()
        if group.startswith("push_"):
            return push(group[5:], grads)
        nms = groups[group]
        after = []
        if group == "in":
            small.update(gathered=_allgather8_wait(*small["gathering"], grads, "gather_small_grads_wait"))
            after = [small["gathered"]]
        full = [g.reshape(N_CHIP, row_sharded[nm], d) if nm in row_sharded else _blocked(g, N_CHIP)
                for g, nm in zip(grads, nms)]
        sems_, full, lands_, token_ = _send_half_start(full, "grads_to_sibling_start_" + group, after)
        to_sibling[group] = (sems_, full, lands_)
        return token_[0, 0]

    def push(group, after):
        nms = groups[group]
        sems_, full, lands_ = to_sibling[group]
        if group == "in":
            after = list(after) + [small["g_c_ctx"], small["total"]]
        full, from_sibling = _send_half_wait(sems_, full, lands_, after, "grads_to_sibling_wait_" + group)
        pairs = [_pair_sum(c_arr, f, r_, "pair_sum_" + nm) for f, r_, nm in zip(full, from_sibling, nms)]
        after = [small["g_c_ctx"], small["total"]] if group == "in" else []
        sems_, pairs, lands_, token_ = _scatter_start(pairs, "grads_to_owner_start_" + group, after)
        in_flight[group] = (sems_, pairs, lands_, token_)
        return token_[0, 0]

    r = _sample_back(reduce, front, x[0], ctx[0], loss_target[0], modc, modx, norm_pre1, norms, onorms, lb_full,
                     gla_side, w_in_r, wbh, wbg, wout, ffn_weights)
    grad_x = r["grad_x"]

    weights = dict(w_in=(w_in, m_w_in, v_w_in), w_br_hg=(w_br_hg, m_w_br_hg, v_w_br_hg),
                   w_br_gla=(w_br_gla, m_w_br_gla, v_w_br_gla), w_out=(w_out, m_w_out, v_w_out),
                   w_ff_gate=(w_ff_gate, m_w_ff_gate, v_w_ff_gate), w_ff_up=(w_ff_up, m_w_ff_up, v_w_ff_up),
                   w_ff_down=(w_ff_down, m_w_ff_down, v_w_ff_down))
    names = ["w_in", "w_br_hg", "w_br_gla", "w_out", "w_ff_gate", "w_ff_up", "w_ff_down"]
    big, swapping = {}, {}

    def sum_and_swap(group, after):
        sems_, pairs, lands_, _ = in_flight[group]
        pairs, lands_ = _scatter_wait(sems_, pairs, lands_, after, "grads_to_owner_wait_" + group)
        own_half = [_sum_owner(chip_arr, pr, g, "chip_sum_" + nm) for pr, g, nm in zip(pairs, lands_, groups[group])]
        swapping[group] = _swap_start(own_half, "halves_to_sibling_start_" + group)
        return own_half[-1]

    def update(group, after):
        sems_, own_half, lands_ = swapping[group]
        own_half, other_half = _swap_wait(sems_, own_half, lands_, after, "halves_to_sibling_wait_" + group)
        done = []
        for nm, own, oth in zip(groups[group], own_half, other_half):
            w_, m_, v_ = (view(a, nm) for a in weights[nm])
            res = _adamw_halves(c_arr, own, oth, w_, m_, v_, "adamw_" + nm)
            big[nm] = [r_.T[None] if nm in transposed else r_[None] for r_ in res]
            done.append(res[1])
        return done

    token_in = in_flight["in"][3]
    summed_ffn = sum_and_swap("ffn", [token_in])
    summed_mix = sum_and_swap("mix", [summed_ffn])

    total, g_b_mod, g_lb_full, g_w_mod, g_c_ctx = (small[k] for k in ("total", "g_b_mod", "g_lb_full", "g_w_mod",
                                                                      "g_c_ctx"))
    g_pre1, g_post1, g_pre2, g_post2 = (total[r_:r_ + 1] for r_ in (ROW_PRE1, ROW_POST1, ROW_PRE2, ROW_POST2))
    g_hg_on, g_gla_on = total[ROW_ONORM:ROW_ONORM + 1, 0:HD], total[ROW_ONORM:ROW_ONORM + 1, HD:2 * HD]
    n_lb = hg_lb.shape[2]
    g_hg_lb = lax.dynamic_slice(g_lb_full, (0, chip * n_lb), (4, n_lb))
    g_bgk = lax.dynamic_slice(total[ROW_BGK:ROW_BGK + 1].reshape(2, HW), (0, chip * n_lb), (2, n_lb))
    g_wgk_full = total[ROW_WGK:ROW_WGK + RANK].reshape(RANK, 2, HW).transpose(1, 0, 2).reshape(2 * RANK, HW)
    g_wgk = lax.dynamic_slice(g_wgk_full, (0, chip * n_lb), (2 * RANK, n_lb))

    small_items = [
        (g_c_ctx, c_ctx.reshape(1, d), m_c_ctx.reshape(1, d), v_c_ctx.reshape(1, d)),
        (g_b_mod, b_mod, m_b_mod, v_b_mod),
        (g_pre1, norm_pre1, m_norm_pre1, v_norm_pre1),
        (g_post1, norm_post1, m_norm_post1, v_norm_post1),
        (g_pre2, norm_pre2, m_norm_pre2, v_norm_pre2),
        (g_post2, norm_post2, m_norm_post2, v_norm_post2),
        (g_hg_lb, hg_lb.reshape(4, n_lb), m_hg_lb.reshape(4, n_lb), v_hg_lb.reshape(4, n_lb)),
        (g_hg_on, hg_onorm, m_hg_onorm, v_hg_onorm),
        (g_wgk, gla_w_gk.reshape(2 * RANK, n_lb), m_gla_w_gk.reshape(2 * RANK, n_lb), v_gla_w_gk.reshape(2 * RANK, n_lb)),
        (g_bgk, gla_b_gk.reshape(2, n_lb), m_gla_b_gk.reshape(2, n_lb), v_gla_b_gk.reshape(2, n_lb)),
        (g_gla_on, gla_onorm, m_gla_onorm, v_gla_onorm),
    ]
    small_res = _adamw_whole(small_items, "adamw_small")
    mod_res = _adamw_tiled(g_w_mod, w_mod[0], m_w_mod[0], v_w_mod[0], "adamw_w_mod")
    done_ffn = update("ffn", [summed_mix, mod_res[0], small_res[0][0]])
    done_mix = update("mix", done_ffn)
    update("in", [sum_and_swap("in", done_mix)])

    loss = total[ROW_LOSS, 0]

    shapes = dict(c_ctx=c_ctx.shape, b_mod=b_mod.shape, norm_pre1=norm_pre1.shape, norm_post1=norm_post1.shape,
                  norm_pre2=norm_pre2.shape, norm_post2=norm_post2.shape, hg_lb=hg_lb.shape, hg_onorm=hg_onorm.shape,
                  gla_w_gk=gla_w_gk.shape, gla_b_gk=gla_b_gk.shape, gla_onorm=gla_onorm.shape)
    small_names = ["c_ctx", "b_mod", "norm_pre1", "norm_post1", "norm_pre2", "norm_post2", "hg_lb", "hg_onorm",
                   "gla_w_gk", "gla_b_gk", "gla_onorm"]
    grads, deltas, new_m, new_v = {}, {}, {}, {}
    for nm, item, res in zip(small_names, small_items, small_res):
        grads[nm] = item[0].reshape(shapes[nm])
        deltas[nm], new_m[nm], new_v[nm] = (r_.reshape(shapes[nm]) for r_ in res)
    grads["w_mod"] = g_w_mod[None]
    deltas["w_mod"], new_m["w_mod"], new_v["w_mod"] = (r_[None] for r_ in mod_res)
    for nm in names:
        grads[nm], deltas[nm], new_m[nm], new_v[nm] = big[nm]
    order = ["c_ctx", "w_mod", "b_mod", "norm_pre1", "norm_post1", "norm_pre2", "norm_post2", "w_in", "hg_lb",
             "hg_onorm", "gla_w_gk", "gla_b_gk", "gla_onorm", "w_br_hg", "w_br_gla", "w_out", "w_ff_gate", "w_ff_up",
             "w_ff_down"]
    return (loss, grad_x[None], *[grads[n] for n in order], *[deltas[n] for n in order],
            *[new_m[n] for n in order], *[new_v[n] for n in order])
```

```python
import functools

import jax
import jax.numpy as jnp
from jax import lax
from jax.experimental import pallas as pl
from jax.experimental.pallas import tpu as pltpu

F32 = jnp.float32
BF16 = jnp.bfloat16
MESH = pl.DeviceIdType.MESH

EPS = 1e-6
CHUNK = 64
SUB = 16
NSUB = CHUNK // SUB
NH = 4
HD = 128
HW = NH * HD
RANK = 16
GATE_NORM = 16.0
N_MOD = 6
TM = 256
BWD_LANES = 8
N_DEV = 8
N_CHIP = 4
VMEM_LIMIT = 56 * 1024 * 1024

ADAM_LR = 0.001
ADAM_B1 = 0.9
ADAM_B2 = 0.999
ADAM_EPS = 1e-08
ADAM_WD = 0.01
ADAM_STEP = 10

VMEM_SPEC = pl.BlockSpec(memory_space=pltpu.VMEM)
ANY_SPEC = pl.BlockSpec(memory_space=pl.ANY)
HBM_SPEC = pl.BlockSpec(memory_space=pltpu.HBM)
SEM_SPEC = pl.BlockSpec(memory_space=pltpu.SEMAPHORE)
EFFECT = pltpu.SideEffectType.DATAFLOW_SIDE_EFFECTING


def _cparams(**kw):
    return pltpu.CompilerParams(vmem_limit_bytes=VMEM_LIMIT, **kw)


def _dot(a, b):
    return jnp.dot(a.astype(BF16), b.astype(BF16), preferred_element_type=F32)


def _dot_nt(a, b):
    return lax.dot_general(a.astype(BF16), b.astype(BF16), (((1,), (1,)), ((), ())), preferred_element_type=F32)


def _dot_tn(a, b):
    return lax.dot_general(a.astype(BF16), b.astype(BF16), (((0,), (0,)), ((), ())), preferred_element_type=F32)


def _sigmoid(x):
    return 1.0 / (1.0 + jnp.exp(-x))


def _silu(x):
    return x * _sigmoid(x)


def _dsilu(x):
    s = _sigmoid(x)
    return s * (1.0 + x * (1.0 - s))


def _log_sigmoid(x):
    return jnp.minimum(x, 0.0) - jnp.log(1.0 + jnp.exp(-jnp.abs(x)))


def _colsum(a):
    return jnp.sum(a, axis=0, keepdims=True)


def _rms(a):
    r = lax.rsqrt(jnp.mean(a * a, axis=-1, keepdims=True) + EPS)
    return a * r, r


def _rms_bwd(dn, n, r):
    return r * (dn - n * jnp.mean(dn * n, axis=-1, keepdims=True))


def _place():
    x, y, c = lax.axis_index("x"), lax.axis_index("y"), lax.axis_index("c")
    chips = [(1 - x, y), (x, 1 - y), (1 - x, 1 - y)]
    return x, y, c, chips


def _allgather8(v, name, after=()):
    rows, cols = v.shape
    n_after = len(after)

    def body(x_ref, *rest):
        out_ref, send_sems, recv_sems, local_sem = rest[n_after:]
        x, y, c, chips = _place()
        me, sibling = (x, y, c), (x, y, 1 - c)

        def blk(px, py, pc):
            return out_ref.at[4 * px + 2 * py + pc]

        def copy(k, block, to, src=None):
            return pltpu.make_async_remote_copy(
                src_ref=blk(*block) if src is None else src, dst_ref=blk(*block),
                send_sem=send_sems.at[k], recv_sem=recv_sems.at[k], device_id=to, device_id_type=MESH)

        mine = pltpu.make_async_copy(x_ref, blk(*me), local_sem)
        mine.start()
        first = [copy(0, me, sibling, src=x_ref)]
        first += [copy(1 + j, me, (*chip, c), src=x_ref) for j, chip in enumerate(chips)]
        for cp in first:
            cp.start()
        passed = [copy(4 + j, (*chip, c), sibling) for j, chip in enumerate(chips)]
        for j, chip in enumerate(chips):
            copy(1 + j, (*chip, c), me).wait_recv()
            passed[j].start()
        copy(0, sibling, me).wait_recv()
        for j, chip in enumerate(chips):
            copy(4 + j, (*chip, 1 - c), me).wait_recv()
        for cp in first + passed:
            cp.wait_send()
        mine.wait()

    return pl.pallas_call(
        body, name=name,
        out_shape=jax.ShapeDtypeStruct((N_DEV, rows, cols), v.dtype),
        in_specs=[VMEM_SPEC] + [ANY_SPEC] * n_after, out_specs=VMEM_SPEC,
        scratch_shapes=[pltpu.SemaphoreType.DMA((7,)), pltpu.SemaphoreType.DMA((7,)), pltpu.SemaphoreType.DMA],
    )(v, *after)


def _allgather8_start(v, name):
    rows, cols = v.shape

    def body(x_ref, out_ref, *rest):
        send_sems, recv_sems, token, local_sem = rest[:4], rest[4:8], rest[-2], rest[-1]
        x, y, c, chips = _place()
        own = out_ref.at[4 * x + 2 * y + c]
        mine = pltpu.make_async_copy(x_ref, own, local_sem)
        mine.start()
        mine.wait()
        for k, to in enumerate([(x, y, 1 - c)] + [(*chip, c) for chip in chips]):
            pltpu.make_async_remote_copy(src_ref=own, dst_ref=own, send_sem=send_sems[k], recv_sem=recv_sems[k],
                                         device_id=to, device_id_type=MESH).start()
        token[...] = jnp.zeros_like(token)

    land = _hbm(lax.empty((N_DEV, rows, cols), v.dtype))
    out = pl.pallas_call(
        body, name=name,
        out_shape=(*[pltpu.SemaphoreType.DMA(())] * 8, pltpu.HBM(land.shape, land.dtype),
                   jax.ShapeDtypeStruct((8, 128), F32)),
        in_specs=[VMEM_SPEC, HBM_SPEC],
        out_specs=(*[SEM_SPEC] * 8, HBM_SPEC, VMEM_SPEC),
        input_output_aliases={1: 8},
        scratch_shapes=[pltpu.SemaphoreType.DMA],
        compiler_params=pltpu.CompilerParams(has_side_effects=EFFECT),
    )(v, land)
    return list(out[:8]), out[8], out[9]


def _allgather8_wait(sems, land, after, name):
    def blk(ref, px, py, pc):
        return ref.at[4 * px + 2 * py + pc]

    def wait_body(out_ref, *rest):
        send_sems, recv_sems = rest[:4], rest[4:8]
        x, y, c, chips = _place()
        me = (x, y, c)
        for k, peer in enumerate([(x, y, 1 - c)] + [(*chip, c) for chip in chips]):
            sent = pltpu.make_async_remote_copy(
                src_ref=blk(out_ref, *me), dst_ref=blk(out_ref, *me), send_sem=send_sems[k], recv_sem=recv_sems[k],
                device_id=peer, device_id_type=MESH)
            sent.wait_send()
            pltpu.make_async_remote_copy(
                src_ref=blk(out_ref, *peer), dst_ref=blk(out_ref, *peer), send_sem=send_sems[k],
                recv_sem=recv_sems[k], device_id=me, device_id_type=MESH).wait_recv()

    def pass_body(out_ref, _, send_sems, recv_sems):
        x, y, c, chips = _place()
        started = []
        for j, chip in enumerate(chips):
            cp = pltpu.make_async_remote_copy(
                src_ref=blk(out_ref, *chip, c), dst_ref=blk(out_ref, *chip, c), send_sem=send_sems.at[j],
                recv_sem=recv_sems.at[j], device_id=(x, y, 1 - c), device_id_type=MESH)
            cp.start()
            started.append(cp)
        for j, chip in enumerate(chips):
            pltpu.make_async_remote_copy(
                src_ref=blk(out_ref, *chip, 1 - c), dst_ref=blk(out_ref, *chip, 1 - c), send_sem=send_sems.at[j],
                recv_sem=recv_sems.at[j], device_id=(x, y, c), device_id_type=MESH).wait_recv()
        for cp in started:
            cp.wait_send()

    land = pl.pallas_call(
        wait_body, name=name,
        out_shape=pltpu.HBM(land.shape, land.dtype),
        in_specs=[HBM_SPEC] + [SEM_SPEC] * 8 + [ANY_SPEC] * len(after),
        out_specs=HBM_SPEC,
        input_output_aliases={0: 0},
        compiler_params=pltpu.CompilerParams(has_side_effects=EFFECT),
    )(land, *sems, *after)
    return pl.pallas_call(
        pass_body, name=name + "_pass",
        out_shape=jax.ShapeDtypeStruct(land.shape, land.dtype),
        in_specs=[ANY_SPEC], out_specs=ANY_SPEC,
        input_output_aliases={0: 0},
        scratch_shapes=[pltpu.SemaphoreType.DMA((3,)), pltpu.SemaphoreType.DMA((3,))],
    )(land)


def _cast_into_blocks(chip_arr, w, name):
    rows, cols = w.shape
    tr = _row_tile(rows, 16, 256)

    def body(chip_ref, w_ref, o_ref):
        o_ref[0] = w_ref[...].astype(BF16)

    return pl.pallas_call(
        body, name=name,
        grid_spec=pltpu.PrefetchScalarGridSpec(
            num_scalar_prefetch=1, grid=(rows // tr,),
            in_specs=[pl.BlockSpec((tr, cols), lambda i, chip_ref: (i, 0))],
            out_specs=pl.BlockSpec((1, tr, cols), lambda i, chip_ref: (chip_ref[0], i, 0))),
        out_shape=jax.ShapeDtypeStruct((N_CHIP, rows, cols), BF16),
        compiler_params=_cparams(dimension_semantics=("parallel",)),
    )(chip_arr, w)


def _halved_by_rows(shape):
    return (shape[1] // 2) % 16 == 0


def _half_of(ref, pc, block=None):
    lead = slice(None) if block is None else block
    if _halved_by_rows(ref.shape):
        h = ref.shape[1] // 2
        return ref.at[lead, pl.ds(pl.multiple_of(pc * h, 16), h), :]
    h = ref.shape[2] // 2
    return ref.at[lead, :, pl.ds(pl.multiple_of(pc * h, 128), h)]


def _half_shape(shape):
    return (shape[0], shape[1] // 2, shape[2]) if _halved_by_rows(shape) else (shape[0], shape[1], shape[2] // 2)


def _half_rows(ref, chip_id, pc):
    return _half_of(ref, pc, chip_id)


def _hbm(a):
    return pltpu.with_memory_space_constraint(a, pltpu.HBM)


def _blocks_start(lands, name, after=()):
    n = len(lands)
    n_sem = 3 * n
    first = n + len(after)

    def body(*refs):
        lnd = refs[:n]
        send_sems, recv_sems = refs[first:first + n_sem], refs[first + n_sem:first + 2 * n_sem]
        token = refs[-1]
        x, y, c, chips = _place()
        me_chip = 2 * x + y
        for k in range(n):
            for j, chip in enumerate(chips):
                pltpu.make_async_remote_copy(
                    src_ref=_half_rows(lnd[k], me_chip, c), dst_ref=_half_rows(lnd[k], me_chip, c),
                    send_sem=send_sems[3 * k + j], recv_sem=recv_sems[3 * k + j],
                    device_id=(*chip, c), device_id_type=MESH).start()
        token[...] = jnp.zeros_like(token)

    out = pl.pallas_call(
        body, name=name,
        out_shape=(*[pltpu.SemaphoreType.DMA(())] * (2 * n_sem),
                   *[pltpu.HBM(l.shape, l.dtype) for l in lands],
                   jax.ShapeDtypeStruct((8, 128), F32)),
        in_specs=[HBM_SPEC] * n + [ANY_SPEC] * len(after),
        out_specs=(*[SEM_SPEC] * (2 * n_sem), *[HBM_SPEC] * n, VMEM_SPEC),
        input_output_aliases={i: 2 * n_sem + i for i in range(n)},
        compiler_params=pltpu.CompilerParams(has_side_effects=EFFECT),
    )(*[_hbm(l) for l in lands], *after)
    return list(out[:2 * n_sem]), list(out[2 * n_sem:2 * n_sem + n]), out[-1]


def _blocks_wait(sems, lands, after, name):
    n = len(lands)
    n_sem = 3 * n

    def body(*refs):
        lnd = refs[:n]
        s_sems, r_sems = refs[n:n + n_sem], refs[n + n_sem:n + 2 * n_sem]
        x, y, c, chips = _place()
        me_chip = 2 * x + y
        for k in range(n):
            for j, (px, py) in enumerate(chips):
                cp = pltpu.make_async_remote_copy(
                    src_ref=_half_rows(lnd[k], me_chip, c), dst_ref=_half_rows(lnd[k], 2 * px + py, c),
                    send_sem=s_sems[3 * k + j], recv_sem=r_sems[3 * k + j],
                    device_id=(px, py, c), device_id_type=MESH)
                cp.wait_send()
                cp.wait_recv()

    out = pl.pallas_call(
        body, name=name,
        out_shape=tuple(pltpu.HBM(l.shape, l.dtype) for l in lands),
        in_specs=[HBM_SPEC] * n + [SEM_SPEC] * (2 * n_sem) + [ANY_SPEC] * len(after),
        out_specs=[HBM_SPEC] * n,
        input_output_aliases={i: i for i in range(n)},
        compiler_params=pltpu.CompilerParams(has_side_effects=EFFECT),
    )(*lands, *sems, *after)
    return list(out)


def _forward_start(lands, name):
    n = len(lands)
    n_sem = 3 * n

    def body(*refs):
        lnd = refs[:n]
        send_sems, recv_sems = refs[n:n + n_sem], refs[n + n_sem:n + 2 * n_sem]
        x, y, c, chips = _place()
        for k in range(n):
            for j, (px, py) in enumerate(chips):
                pltpu.make_async_remote_copy(
                    src_ref=_half_rows(lnd[k], 2 * px + py, c), dst_ref=_half_rows(lnd[k], 2 * px + py, c),
                    send_sem=send_sems[3 * k + j], recv_sem=recv_sems[3 * k + j],
                    device_id=(x, y, 1 - c), device_id_type=MESH).start()
        refs[-1][...] = jnp.zeros_like(refs[-1])

    out = pl.pallas_call(
        body, name=name,
        out_shape=(*[pltpu.SemaphoreType.DMA(())] * (2 * n_sem), *[pltpu.HBM(l.shape, l.dtype) for l in lands],
                   jax.ShapeDtypeStruct((8, 128), F32)),
        in_specs=[HBM_SPEC] * n,
        out_specs=(*[SEM_SPEC] * (2 * n_sem), *[HBM_SPEC] * n, VMEM_SPEC),
        input_output_aliases={i: 2 * n_sem + i for i in range(n)},
        compiler_params=pltpu.CompilerParams(has_side_effects=EFFECT),
    )(*[_hbm(l) for l in lands])
    return list(out[:2 * n_sem]), list(out[2 * n_sem:2 * n_sem + n]), out[-1]


def _forward_wait(sems, lands, after, name):
    n = len(lands)
    n_sem = 3 * n

    def body(*refs):
        lnd = refs[:n]
        s_sems, r_sems = refs[n:n + n_sem], refs[n + n_sem:n + 2 * n_sem]
        x, y, c, chips = _place()
        for k in range(n):
            for j, (px, py) in enumerate(chips):
                cp = pltpu.make_async_remote_copy(
                    src_ref=_half_rows(lnd[k], 2 * px + py, c), dst_ref=_half_rows(lnd[k], 2 * px + py, 1 - c),
                    send_sem=s_sems[3 * k + j], recv_sem=r_sems[3 * k + j],
                    device_id=(x, y, 1 - c), device_id_type=MESH)
                cp.wait_send()
                cp.wait_recv()

    out = pl.pallas_call(
        body, name=name,
        out_shape=tuple(pltpu.HBM(l.shape, l.dtype) for l in lands),
        in_specs=[HBM_SPEC] * n + [SEM_SPEC] * (2 * n_sem) + [ANY_SPEC] * len(after),
        out_specs=[HBM_SPEC] * n,
        input_output_aliases={i: i for i in range(n)},
        compiler_params=pltpu.CompilerParams(has_side_effects=EFFECT),
    )(*lands, *sems, *after)
    return list(out)


def _blocks_finish(lands, name):
    n = len(lands)

    def body(*refs):
        lnd = refs[n:2 * n]
        send_sems, recv_sems = refs[2 * n:]
        x, y, c, chips = _place()
        sibling = (x, y, 1 - c)

        def copy(k, j, chip_id, pc):
            return pltpu.make_async_remote_copy(
                src_ref=_half_rows(lnd[k], chip_id, pc), dst_ref=_half_rows(lnd[k], chip_id, pc),
                send_sem=send_sems.at[k, j], recv_sem=recv_sems.at[k, j], device_id=sibling, device_id_type=MESH)

        started = []
        for k in range(n):
            for j, (px, py) in enumerate(chips):
                cp = copy(k, j, 2 * px + py, c)
                cp.start()
                started.append(cp)
        for k in range(n):
            for j, (px, py) in enumerate(chips):
                copy(k, j, 2 * px + py, 1 - c).wait_recv()
        for cp in started:
            cp.wait_send()

    out = pl.pallas_call(
        body, name=name,
        out_shape=[jax.ShapeDtypeStruct(l.shape, l.dtype) for l in lands],
        in_specs=[ANY_SPEC] * n, out_specs=[ANY_SPEC] * n,
        input_output_aliases={i: i for i in range(n)},
        scratch_shapes=[pltpu.SemaphoreType.DMA((n, 3)), pltpu.SemaphoreType.DMA((n, 3))],
    )(*lands)
    return list(out)


def _send_half_start(arrs, name, after=()):
    n = len(arrs)
    first = 2 * n + len(after)

    def body(*refs):
        ins, lnd = refs[:n], refs[n:2 * n]
        send_sems, recv_sems = refs[first:first + n], refs[first + n:first + 2 * n]
        token = refs[-1]
        x, y, c, _ = _place()
        for k in range(n):
            pltpu.make_async_remote_copy(
                src_ref=_half_of(ins[k], 1 - c), dst_ref=lnd[k], send_sem=send_sems[k], recv_sem=recv_sems[k],
                device_id=(x, y, 1 - c), device_id_type=MESH).start()
        token[...] = jnp.zeros_like(token)

    lands = [_hbm(lax.empty(_half_shape(a.shape), a.dtype)) for a in arrs]
    out = pl.pallas_call(
        body, name=name,
        out_shape=(*[pltpu.SemaphoreType.DMA(())] * (2 * n), *[pltpu.HBM(a.shape, a.dtype) for a in arrs],
                   *[pltpu.HBM(l.shape, l.dtype) for l in lands], jax.ShapeDtypeStruct((8, 128), F32)),
        in_specs=[HBM_SPEC] * (2 * n) + [ANY_SPEC] * len(after),
        out_specs=(*[SEM_SPEC] * (2 * n), *[HBM_SPEC] * (2 * n), VMEM_SPEC),
        input_output_aliases={i: 2 * n + i for i in range(2 * n)},
        compiler_params=pltpu.CompilerParams(has_side_effects=EFFECT),
    )(*[_hbm(a) for a in arrs], *lands, *after)
    return list(out[:2 * n]), list(out[2 * n:3 * n]), list(out[3 * n:4 * n]), out[-1]


def _send_half_wait(sems, arrs, lands, after, name):
    n = len(arrs)

    def body(*refs):
        ins, lnd = refs[:n], refs[n:2 * n]
        s_sems, r_sems = refs[2 * n:3 * n], refs[3 * n:4 * n]
        x, y, c, _ = _place()
        for k in range(n):
            cp = pltpu.make_async_remote_copy(
                src_ref=_half_of(ins[k], 1 - c), dst_ref=lnd[k], send_sem=s_sems[k], recv_sem=r_sems[k],
                device_id=(x, y, 1 - c), device_id_type=MESH)
            cp.wait_send()
            cp.wait_recv()

    out = pl.pallas_call(
        body, name=name,
        out_shape=tuple(pltpu.HBM(a.shape, a.dtype) for a in list(arrs) + list(lands)),
        in_specs=[HBM_SPEC] * (2 * n) + [SEM_SPEC] * (2 * n) + [ANY_SPEC] * len(after),
        out_specs=[HBM_SPEC] * (2 * n),
        input_output_aliases={i: i for i in range(2 * n)},
        compiler_params=pltpu.CompilerParams(has_side_effects=EFFECT),
    )(*arrs, *lands, *sems, *after)
    return list(out[:n]), list(out[n:])


def _scatter_start(arrs, name, after=()):
    n = len(arrs)
    n_sem = 3 * n
    first = 2 * n + len(after)

    def body(*refs):
        ins, lnd = refs[:n], refs[n:2 * n]
        send_sems, recv_sems = refs[first:first + n_sem], refs[first + n_sem:first + 2 * n_sem]
        token = refs[-1]
        x, y, c, chips = _place()
        me_chip = 2 * x + y
        for k in range(n):
            for j, (px, py) in enumerate(chips):
                pltpu.make_async_remote_copy(
                    src_ref=ins[k].at[2 * px + py], dst_ref=lnd[k].at[me_chip],
                    send_sem=send_sems[3 * k + j], recv_sem=recv_sems[3 * k + j],
                    device_id=(px, py, c), device_id_type=MESH).start()
        token[...] = jnp.zeros_like(token)

    lands = [_hbm(lax.empty(a.shape, a.dtype)) for a in arrs]
    out = pl.pallas_call(
        body, name=name,
        out_shape=(*[pltpu.SemaphoreType.DMA(())] * (2 * n_sem),
                   *[pltpu.HBM(a.shape, a.dtype) for a in arrs], *[pltpu.HBM(a.shape, a.dtype) for a in arrs],
                   jax.ShapeDtypeStruct((8, 128), F32)),
        in_specs=[HBM_SPEC] * (2 * n) + [ANY_SPEC] * len(after),
        out_specs=(*[SEM_SPEC] * (2 * n_sem), *[HBM_SPEC] * (2 * n), VMEM_SPEC),
        input_output_aliases={i: 2 * n_sem + i for i in range(2 * n)},
        compiler_params=pltpu.CompilerParams(has_side_effects=EFFECT),
    )(*[_hbm(a) for a in arrs], *lands, *after)
    base = 2 * n_sem
    return list(out[:base]), list(out[base:base + n]), list(out[base + n:base + 2 * n]), out[-1]


def _scatter_wait(sems, arrs, lands, after, name):
    n = len(arrs)
    n_sem = 3 * n

    def body(*refs):
        ins, lnd = refs[:n], refs[n:2 * n]
        s_sems, r_sems = refs[2 * n:2 * n + n_sem], refs[2 * n + n_sem:2 * n + 2 * n_sem]
        x, y, c, chips = _place()
        for k in range(n):
            for j, (px, py) in enumerate(chips):
                cp = pltpu.make_async_remote_copy(
                    src_ref=ins[k].at[2 * px + py], dst_ref=lnd[k].at[2 * px + py],
                    send_sem=s_sems[3 * k + j], recv_sem=r_sems[3 * k + j],
                    device_id=(px, py, c), device_id_type=MESH)
                cp.wait_send()
                cp.wait_recv()

    out = pl.pallas_call(
        body, name=name,
        out_shape=tuple(pltpu.HBM(a.shape, a.dtype) for a in list(arrs) + list(lands)),
        in_specs=[HBM_SPEC] * (2 * n) + [SEM_SPEC] * (2 * n_sem) + [ANY_SPEC] * len(after),
        out_specs=[HBM_SPEC] * (2 * n),
        input_output_aliases={i: i for i in range(2 * n)},
        compiler_params=pltpu.CompilerParams(has_side_effects=EFFECT),
    )(*arrs, *lands, *sems, *after)
    return list(out[:n]), list(out[n:])


def _sum_owner(chip_arr, pairs, got, name):
    nb, h, cols = got.shape
    tr = _row_tile(h, 16, 256)

    def body(chip_ref, own_ref, a_ref, b_ref, c_ref, o_ref):
        o_ref[...] = ((own_ref[0].astype(F32) + a_ref[0].astype(F32)) + b_ref[0].astype(F32)) + c_ref[0].astype(F32)

    def slot(off):
        return pl.BlockSpec((1, tr, cols), lambda i, chip_ref: ((chip_ref[0] + off) % N_CHIP, i, 0))

    return pl.pallas_call(
        body, name=name,
        grid_spec=pltpu.PrefetchScalarGridSpec(
            num_scalar_prefetch=1, grid=(h // tr,),
            in_specs=[slot(0), slot(1), slot(2), slot(3)],
            out_specs=pl.BlockSpec((tr, cols), lambda i, chip_ref: (i, 0))),
        out_shape=jax.ShapeDtypeStruct((h, cols), F32),
        compiler_params=_cparams(dimension_semantics=("parallel",)),
    )(chip_arr, pairs, got, got, got)


def _swap_start(arrs, name, after=()):
    n = len(arrs)
    first = 2 * n + len(after)

    def body(*refs):
        ins, lnd = refs[:n], refs[n:2 * n]
        send_sems, recv_sems = refs[first:first + n], refs[first + n:first + 2 * n]
        x, y, c, _ = _place()
        for k in range(n):
            pltpu.make_async_remote_copy(
                src_ref=ins[k], dst_ref=lnd[k], send_sem=send_sems[k], recv_sem=recv_sems[k],
                device_id=(x, y, 1 - c), device_id_type=MESH).start()

    lands = [_hbm(lax.empty(a.shape, a.dtype)) for a in arrs]
    out = pl.pallas_call(
        body, name=name,
        out_shape=(*[pltpu.SemaphoreType.DMA(())] * (2 * n), *[pltpu.HBM(a.shape, a.dtype) for a in arrs],
                   *[pltpu.HBM(a.shape, a.dtype) for a in arrs]),
        in_specs=[HBM_SPEC] * (2 * n) + [ANY_SPEC] * len(after),
        out_specs=(*[SEM_SPEC] * (2 * n), *[HBM_SPEC] * (2 * n)),
        input_output_aliases={i: 2 * n + i for i in range(2 * n)},
        compiler_params=pltpu.CompilerParams(has_side_effects=EFFECT),
    )(*[_hbm(a) for a in arrs], *lands, *after)
    return list(out[:2 * n]), list(out[2 * n:3 * n]), list(out[3 * n:4 * n])


def _swap_wait(sems, arrs, lands, after, name):
    n = len(arrs)

    def body(*refs):
        ins, lnd = refs[:n], refs[n:2 * n]
        s_sems, r_sems = refs[2 * n:3 * n], refs[3 * n:4 * n]
        x, y, c, _ = _place()
        for k in range(n):
            cp = pltpu.make_async_remote_copy(
                src_ref=ins[k], dst_ref=lnd[k], send_sem=s_sems[k], recv_sem=r_sems[k],
                device_id=(x, y, 1 - c), device_id_type=MESH)
            cp.wait_send()
            cp.wait_recv()

    out = pl.pallas_call(
        body, name=name,
        out_shape=tuple(pltpu.HBM(a.shape, a.dtype) for a in list(arrs) + list(lands)),
        in_specs=[HBM_SPEC] * (2 * n) + [SEM_SPEC] * (2 * n) + [ANY_SPEC] * len(after),
        out_specs=[HBM_SPEC] * (2 * n),
        input_output_aliases={i: i for i in range(2 * n)},
        compiler_params=pltpu.CompilerParams(has_side_effects=EFFECT),
    )(*arrs, *lands, *sems, *after)
    return list(out[:n]), list(out[n:])


def _row_tile(h, mult=8, cap=256):
    for t in range(cap - cap % mult, mult - 1, -mult):
        if h % t == 0:
            return t
    if mult > 8:
        return _row_tile(h, 8, cap)
    raise ValueError(h)


def _pair_sum(c_arr, full, recv, name):
    nb, rows, cols = full.shape

    def body(c_ref, f_ref, r_ref, o_ref):
        o_ref[...] = (f_ref[...] + r_ref[...]).astype(BF16)

    if _halved_by_rows(full.shape):
        h = rows // 2
        tr = _row_tile(h, 16, 512)
        steps = h // tr
        own = pl.BlockSpec((1, tr, cols), lambda b, i, c_ref: (b, c_ref[0] * steps + i, 0))
        half = pl.BlockSpec((1, tr, cols), lambda b, i, c_ref: (b, i, 0))
    else:
        steps = 1
        own = pl.BlockSpec((1, rows, cols // 2), lambda b, i, c_ref: (b, 0, c_ref[0]))
        half = pl.BlockSpec((1, rows, cols // 2), lambda b, i, c_ref: (b, 0, 0))
    return pl.pallas_call(
        body, name=name,
        grid_spec=pltpu.PrefetchScalarGridSpec(
            num_scalar_prefetch=1, grid=(nb, steps), in_specs=[own, half], out_specs=half),
        out_shape=jax.ShapeDtypeStruct(_half_shape(full.shape), BF16),
        compiler_params=_cparams(dimension_semantics=("parallel", "parallel")),
    )(c_arr, full, recv)


def _adam_math(g, w, m, v):
    m1 = ADAM_B1 * m + (1.0 - ADAM_B1) * g
    v1 = ADAM_B2 * v + (1.0 - ADAM_B2) * (g * g)
    m_hat = m1 / (1.0 - ADAM_B1 ** ADAM_STEP)
    v_hat = v1 / (1.0 - ADAM_B2 ** ADAM_STEP)
    delta = -ADAM_LR * (m_hat / (jnp.sqrt(v_hat) + ADAM_EPS) + ADAM_WD * w)
    return delta, m1, v1


def _adamw_halves(c_arr, own, other, w, m, v, name):
    rows, cols = w.shape
    by_rows = own.shape[1] == cols

    def body(c_ref, own_ref, oth_ref, w_ref, m_ref, v_ref, g_out, d_out, m_out, v_out):
        if by_rows:
            g = jnp.where(pl.program_id(0) == c_ref[0], own_ref[...], oth_ref[...])
        else:
            own_, oth_ = own_ref[...], oth_ref[...]
            g = jnp.where(c_ref[0] == 0, jnp.concatenate([own_, oth_], axis=1), jnp.concatenate([oth_, own_], axis=1))
        d, m1, v1 = _adam_math(g, w_ref[...], m_ref[...], v_ref[...])
        g_out[...] = g
        d_out[...] = d
        m_out[...] = m1
        v_out[...] = v1

    if by_rows:
        h = rows // 2
        tr = _row_tile(h)
        steps = h // tr
        grid = (2, steps)
        half_spec = pl.BlockSpec((tr, cols), lambda p, i, c_ref: (i, 0))
        full_spec = pl.BlockSpec((tr, cols), lambda p, i, c_ref: (p * steps + i, 0))
    else:
        tr = _row_tile(rows)
        grid = (1, rows // tr)
        half_spec = pl.BlockSpec((tr, cols // 2), lambda p, i, c_ref: (i, 0))
        full_spec = pl.BlockSpec((tr, cols), lambda p, i, c_ref: (i, 0))
    return pl.pallas_call(
        body, name=name,
        grid_spec=pltpu.PrefetchScalarGridSpec(
            num_scalar_prefetch=1, grid=grid,
            in_specs=[half_spec, half_spec, full_spec, full_spec, full_spec],
            out_specs=[full_spec] * 4),
        out_shape=[jax.ShapeDtypeStruct(w.shape, F32)] * 4,
        compiler_params=_cparams(dimension_semantics=("parallel", "parallel")),
    )(c_arr, own, other, w, m, v)


def _adamw_whole(items, name):
    n = len(items)

    def body(*refs):
        ins, outs = refs[:4 * n], refs[4 * n:]
        for k in range(n):
            g, w, m, v = (r[...] for r in ins[4 * k:4 * k + 4])
            d, m1, v1 = _adam_math(g, w, m, v)
            outs[3 * k][...] = d
            outs[3 * k + 1][...] = m1
            outs[3 * k + 2][...] = v1

    flat = [a for it in items for a in it]
    shapes = [jax.ShapeDtypeStruct(it[1].shape, F32) for it in items for _ in range(3)]
    out = pl.pallas_call(
        body, name=name, out_shape=shapes,
        in_specs=[VMEM_SPEC] * (4 * n), out_specs=[VMEM_SPEC] * (3 * n),
        compiler_params=_cparams(),
    )(*flat)
    return [tuple(out[3 * k:3 * k + 3]) for k in range(n)]


def _adamw_tiled(g, w, m, v, name):
    rows, cols = w.shape
    tr = _row_tile(rows)

    def body(g_ref, w_ref, m_ref, v_ref, d_out, m_out, v_out):
        d, m1, v1 = _adam_math(g_ref[...], w_ref[...], m_ref[...], v_ref[...])
        d_out[...] = d
        m_out[...] = m1
        v_out[...] = v1

    spec = pl.BlockSpec((tr, cols), lambda i: (i, 0))
    return pl.pallas_call(
        body, name=name, grid=(rows // tr,),
        out_shape=[jax.ShapeDtypeStruct(w.shape, F32)] * 3,
        in_specs=[spec] * 4, out_specs=[spec] * 3,
        compiler_params=_cparams(dimension_semantics=("parallel",)),
    )(g, w, m, v)


def _mod_forward(cond, w_mod, b_mod_cols, name):
    def body(c_ref, w_ref, b_ref, o_ref):
        o_ref[...] = _dot(_silu(c_ref[...]), w_ref[...]) + b_ref[...]

    return pl.pallas_call(
        body, name=name, out_shape=jax.ShapeDtypeStruct((cond.shape[0], w_mod.shape[1]), F32),
        in_specs=[VMEM_SPEC] * 3, out_specs=VMEM_SPEC, compiler_params=_cparams(),
    )(cond, w_mod, b_mod_cols)


def _mod_backward(cond, w_mod, dmod_cols, name):
    def body(c_ref, w_ref, d_ref, gw_ref, gc_ref):
        s = _silu(c_ref[...])
        d = d_ref[...]
        gw_ref[...] = _dot_tn(s, d)
        gc_ref[...] = _dot_nt(d[8:16, :], w_ref[...])

    return pl.pallas_call(
        body, name=name,
        out_shape=[jax.ShapeDtypeStruct(w_mod.shape, F32), jax.ShapeDtypeStruct((8, w_mod.shape[0]), F32)],
        in_specs=[VMEM_SPEC] * 3, out_specs=[VMEM_SPEC] * 2, compiler_params=_cparams(),
    )(cond, w_mod, dmod_cols)


def _col_chunks(width, step=512):
    return [(s, min(step, width - s)) for s in range(0, width, step)]


def _w_in_row(p_off):
    if p_off < 9 * HW:
        return p_off
    return 9 * HW if p_off == OFF_LR else p_off + 2 * RANK


def _in_projection(ctx0, x0, modc, modx, pre1, w_t, n_ctx_tiles, name):
    d = x0.shape[1]
    rows = ctx0.shape[0] + x0.shape[0]
    width = P_WIDTH

    def body(ctx_ref, x_ref, modc_ref, modx_ref, pre_ref, w_ref, h_ref, p_ref):
        is_ctx = pl.program_id(0) < n_ctx_tiles
        n, _ = _rms(jnp.where(is_ctx, ctx_ref[...], x_ref[...]))
        shift = jnp.where(is_ctx, modc_ref[0:1, :], modx_ref[0:1, :])
        scale = jnp.where(is_ctx, modc_ref[1:2, :], modx_ref[1:2, :])
        h = (n * pre_ref[...] * (1.0 + scale) + shift).astype(BF16)
        h_ref[...] = h
        for s, w in _col_chunks(width):
            p_ref[:, s:s + w] = _dot_nt(h, w_ref[_w_in_row(s):_w_in_row(s) + w, :])

    row = lambda i: (i, 0)
    fixed = lambda i: (0, 0)
    return pl.pallas_call(
        body, name=name, grid=(rows // TM,),
        out_shape=[jax.ShapeDtypeStruct((rows, d), BF16), jax.ShapeDtypeStruct((rows, width), F32)],
        in_specs=[pl.BlockSpec((TM, d), lambda i: (jnp.minimum(i, n_ctx_tiles - 1), 0)),
                  pl.BlockSpec((TM, d), lambda i: (jnp.maximum(i - n_ctx_tiles, 0), 0)),
                  pl.BlockSpec((8, d), fixed), pl.BlockSpec((8, d), fixed), pl.BlockSpec((1, d), fixed), VMEM_SPEC],
        out_specs=[pl.BlockSpec((TM, d), row), pl.BlockSpec((TM, width), row)],
        compiler_params=_cparams(dimension_semantics=("parallel",)),
    )(ctx0, x0, modc, modx, pre1, w_t)


C_HQ, C_HI, C_HF_FW, C_HF_BW, C_HGATE, C_GQ, C_GK, C_GV, C_GGATE = range(9)
OFF_GATE_HG = 9 * HW
OFF_LR = 13 * HW
P_WIDTH = OFF_LR + 128


def _head_norm_fwd(o, w):
    outs, ns, rs = [], [], []
    for h in range(NH):
        n, r = _rms(o[:, h * HD:(h + 1) * HD])
        ns.append(n)
        rs.append(r)
        outs.append(n * w)
    return jnp.concatenate(outs, axis=1), ns, rs


def _mixer_tail(z, o_hg, o_gla, p_hgate, p_ggate, p_gate_hg, p_gate_gla, hg_on, gla_on, wbh, wbg, wout):
    on_hg, n_hg, r_hg = _head_norm_fwd(o_hg, hg_on)
    on_gla, n_gla, r_gla = _head_norm_fwd(o_gla, gla_on)
    og_hg = (on_hg * _silu(p_hgate)).astype(BF16)
    og_gla = (on_gla * _silu(p_ggate)).astype(BF16)
    b_hg = jnp.dot(og_hg, wbh, preferred_element_type=F32)
    b_gla = jnp.dot(og_gla, wbg, preferred_element_type=F32)
    s_hg = _sigmoid(p_gate_hg)
    s_gla = _sigmoid(p_gate_gla)
    merged = (s_hg * b_hg + s_gla * b_gla).astype(BF16)
    y1 = jnp.dot(merged, wout, preferred_element_type=F32)
    return dict(on_hg=on_hg, n_hg=n_hg, r_hg=r_hg, on_gla=on_gla, n_gla=n_gla, r_gla=r_gla, og_hg=og_hg,
                og_gla=og_gla, b_hg=b_hg, b_gla=b_gla, s_hg=s_hg, s_gla=s_gla, merged=merged, y1=y1)


def _mixer_tail_fwd(x_lat, p, o_list, modx, norms, onorms, w_br_hg, w_br_gla, w_out, n_ctx_tiles, name):
    rows, d = x_lat.shape

    def body(x_ref, ofw_hg, obw_hg, ofw_gla, obw_gla, p_hgate, p_ggate, p_ghg_a, p_ghg_b, p_ggla_a, p_ggla_b,
             modx_ref, norm_ref, on_ref, wbh_ref, wbg_ref, wout_ref, z2_ref, y1_ref, mrg_ref, oghg_ref, oggla_ref):
        p_gate_hg = jnp.concatenate([p_ghg_a[...], p_ghg_b[...]], axis=1)
        p_gate_gla = jnp.concatenate([p_ggla_a[...], p_ggla_b[...]], axis=1)
        t = _mixer_tail(x_ref[...], ofw_hg[...] + obw_hg[...], ofw_gla[...] + obw_gla[...], p_hgate[...],
                        p_ggate[...], p_gate_hg, p_gate_gla, on_ref[0:1, 0:HD], on_ref[1:2, 0:HD],
                        wbh_ref[...], wbg_ref[...], wout_ref[...])
        y1_ref[...] = t["y1"]
        mrg_ref[...] = t["merged"]
        oghg_ref[...] = t["og_hg"]
        oggla_ref[...] = t["og_gla"]
        n1, _ = _rms(t["y1"])
        z2_ref[...] = x_ref[...] + n1 * norm_ref[1:2, :] * modx_ref[2:3, :]

    lat = lambda i: (i, 0)
    full = lambda i: (i + n_ctx_tiles, 0)
    fixed = lambda i: (0, 0)

    def pcol(blk):
        return pl.BlockSpec((TM, HW), lambda i: (i + n_ctx_tiles, blk))

    in_specs = ([pl.BlockSpec((TM, d), lat)] + [pl.BlockSpec((TM, HW), full)] * 4
                + [pcol(C_HGATE), pcol(C_GGATE), pcol(9), pcol(10), pcol(11), pcol(12)]
                + [pl.BlockSpec((8, d), fixed)] * 3 + [VMEM_SPEC] * 3)
    bf = lambda w: jax.ShapeDtypeStruct((rows, w), BF16)
    f32 = jax.ShapeDtypeStruct((rows, d), F32)
    return pl.pallas_call(
        body, name=name, grid=(rows // TM,), out_shape=[f32, f32, bf(d), bf(HW), bf(HW)], in_specs=in_specs,
        out_specs=[pl.BlockSpec((TM, d), lat)] * 3 + [pl.BlockSpec((TM, HW), lat)] * 2,
        compiler_params=_cparams(dimension_semantics=("parallel",)),
    )(x_lat, *o_list, p, p, p, p, p, p, modx, norms, onorms, w_br_hg, w_br_gla, w_out)


def _ffn_fwd_bwd(z2, modx, norms, w_gate, w_up, w_down, target, name):
    rows, d = z2.shape
    dff = w_gate.shape[0]
    inv_d = 1.0 / d

    def body(z2_ref, modx_ref, norm_ref, wg_ref, wu_ref, wd_ref, t_ref,
             loss_ref, dz2_ref, h2_ref, a_ref, du_ref, dv_ref, dy2_ref, stat_ref):
        i = pl.program_id(0)
        pre2, post2 = norm_ref[2:3, :], norm_ref[3:4, :]
        shift2, scale2, gate2 = modx_ref[3:4, :], modx_ref[4:5, :], modx_ref[5:6, :]
        z2 = z2_ref[...]
        n2, r2 = _rms(z2)
        nw2 = n2 * pre2
        h2 = (nw2 * (1.0 + scale2) + shift2).astype(BF16)
        h2_ref[...] = h2
        u = _dot_nt(h2, wg_ref[...])
        v = _dot_nt(h2, wu_ref[...])
        su = _silu(u)
        a = (su * v).astype(BF16)
        a_ref[...] = a
        y2 = jnp.dot(a, wd_ref[...], preferred_element_type=F32)
        n3, r3 = _rms(y2)
        err = z2 + n3 * post2 * gate2 - t_ref[...]
        part = 0.5 * inv_d * jnp.sum(err * err)
        dz3 = err * inv_d
        dgate2 = _colsum(dz3 * n3 * post2)
        tt = dz3 * gate2
        dpost2 = _colsum(tt * n3)
        dy2 = _rms_bwd(tt * post2, n3, r3).astype(BF16)
        dy2_ref[...] = dy2
        da = _dot_nt(dy2, wd_ref[...])
        du = (da * v * _dsilu(u)).astype(BF16)
        dv = (da * su).astype(BF16)
        du_ref[...] = du
        dv_ref[...] = dv
        dh2 = (jnp.dot(du, wg_ref[...], preferred_element_type=F32)
               + jnp.dot(dv, wu_ref[...], preferred_element_type=F32))
        dshift2 = _colsum(dh2)
        dscale2 = _colsum(dh2 * nw2)
        dnw2 = dh2 * (1.0 + scale2)
        dpre2 = _colsum(dnw2 * n2)
        dz2_ref[...] = dz3 + _rms_bwd(dnw2 * pre2, n2, r2)

        @pl.when(i == 0)
        def _():
            stat_ref[...] = jnp.zeros_like(stat_ref)
            loss_ref[...] = jnp.zeros_like(loss_ref)

        for r, val in enumerate((dshift2, dscale2, dgate2, dpre2, dpost2)):
            stat_ref[r:r + 1, :] += val
        loss_ref[...] += part
        stat_ref[5:6, 0:128] += part

    lat = lambda i: (i, 0)
    fixed = lambda i: (0, 0)
    bf = lambda w: jax.ShapeDtypeStruct((rows, w), BF16)
    return pl.pallas_call(
        body, name=name, grid=(rows // TM,),
        out_shape=[jax.ShapeDtypeStruct((8, 128), F32), jax.ShapeDtypeStruct((rows, d), F32), bf(d), bf(dff), bf(dff),
                   bf(dff), bf(d), jax.ShapeDtypeStruct((8, d), F32)],
        in_specs=[pl.BlockSpec((TM, d), lat), pl.BlockSpec((8, d), fixed), pl.BlockSpec((8, d), fixed)]
        + [VMEM_SPEC] * 3 + [pl.BlockSpec((TM, d), lat)],
        out_specs=[pl.BlockSpec((8, 128), fixed), pl.BlockSpec((TM, d), lat), pl.BlockSpec((TM, d), lat),
                   pl.BlockSpec((TM, dff), lat), pl.BlockSpec((TM, dff), lat), pl.BlockSpec((TM, dff), lat),
                   pl.BlockSpec((TM, d), lat), pl.BlockSpec((8, d), fixed)],
        compiler_params=_cparams(dimension_semantics=("arbitrary",)),
    )(z2, modx, norms, w_gate, w_up, w_down, target)


def _mixer_tail_bwd(x_lat, p, o_list, dz2, y1, modx, norms, onorms, w_br_hg, w_br_gla, w_out, n_ctx_tiles, n_tiles,
                    name):
    rows, d = x_lat.shape
    total = n_tiles * TM

    def body(x_ref, ofw_hg, obw_hg, ofw_gla, obw_gla, p_hgate, p_ggate, p_ghg_a, p_ghg_b, p_ggla_a, p_ggla_b,
             dz2_ref, y1_ref, modx_ref, norm_ref, on_ref, wbh_ref, wbg_ref, wout_ref,
             dohg_ref, dogla_ref, dhgate_ref, dggate_ref, dghg_ref, dggla_ref, dy1_ref, dbhg_ref, dbgla_ref,
             stat_ref):
        i = pl.program_id(0)

        @pl.when(i == 0)
        def _():
            stat_ref[...] = jnp.zeros_like(stat_ref)

        @pl.when(i < n_ctx_tiles)
        def _():
            for ref in (dohg_ref, dogla_ref, dhgate_ref, dggate_ref, dghg_ref, dggla_ref):
                ref[...] = jnp.zeros_like(ref)

        @pl.when(i >= n_ctx_tiles)
        def _():
            post1, gate1 = norm_ref[1:2, :], modx_ref[2:3, :]
            hg_on, gla_on = on_ref[0:1, 0:HD], on_ref[1:2, 0:HD]
            p_gate_hg = jnp.concatenate([p_ghg_a[...], p_ghg_b[...]], axis=1)
            p_gate_gla = jnp.concatenate([p_ggla_a[...], p_ggla_b[...]], axis=1)
            ph, pg = p_hgate[...], p_ggate[...]
            t = _mixer_tail(x_ref[...], ofw_hg[...] + obw_hg[...], ofw_gla[...] + obw_gla[...], ph, pg,
                            p_gate_hg, p_gate_gla, hg_on, gla_on, wbh_ref[...], wbg_ref[...], wout_ref[...])
            dz2 = dz2_ref[...]
            n1, r1 = _rms(y1_ref[...])
            dgate1 = _colsum(dz2 * n1 * post1)
            tt = dz2 * gate1
            dpost1 = _colsum(tt * n1)
            dy1 = _rms_bwd(tt * post1, n1, r1).astype(BF16)
            dy1_ref[...] = dy1
            dmerged = _dot_nt(dy1, wout_ref[...])
            dghg_ref[...] = (dmerged * t["b_hg"] * t["s_hg"] * (1.0 - t["s_hg"])).astype(BF16)
            dggla_ref[...] = (dmerged * t["b_gla"] * t["s_gla"] * (1.0 - t["s_gla"])).astype(BF16)
            db_hg = (dmerged * t["s_hg"]).astype(BF16)
            db_gla = (dmerged * t["s_gla"]).astype(BF16)
            dbhg_ref[...] = db_hg
            dbgla_ref[...] = db_gla
            don_acc = []
            for (db, wb, pgate, on, ns, rs, gain, gate_ref, do_ref) in (
                    (db_hg, wbh_ref, ph, t["on_hg"], t["n_hg"], t["r_hg"], hg_on, dhgate_ref, dohg_ref),
                    (db_gla, wbg_ref, pg, t["on_gla"], t["n_gla"], t["r_gla"], gla_on, dggate_ref, dogla_ref)):
                dog = _dot_nt(db, wb[...])
                gate_ref[...] = (dog * on * _dsilu(pgate)).astype(BF16)
                don = dog * _silu(pgate)
                acc = jnp.zeros((1, HD), F32)
                for h in range(NH):
                    sl = slice(h * HD, (h + 1) * HD)
                    acc = acc + _colsum(don[:, sl] * ns[h])
                    do_ref[:, sl] = _rms_bwd(don[:, sl] * gain, ns[h], rs[h]).astype(BF16)
                don_acc.append(acc)
            stat_ref[0:1, :] += dgate1
            stat_ref[1:2, :] += dpost1
            stat_ref[2:3, 0:HD] += don_acc[0]
            stat_ref[2:3, HD:2 * HD] += don_acc[1]

    lat = lambda i: (jnp.maximum(i - n_ctx_tiles, 0), 0)
    full = lambda i: (i, 0)
    fixed = lambda i: (0, 0)

    def pcol(blk):
        return pl.BlockSpec((TM, HW), lambda i: (i, blk))

    in_specs = ([pl.BlockSpec((TM, d), lat)] + [pl.BlockSpec((TM, HW), full)] * 4
                + [pcol(C_HGATE), pcol(C_GGATE), pcol(9), pcol(10), pcol(11), pcol(12)]
                + [pl.BlockSpec((TM, d), lat), pl.BlockSpec((TM, d), lat)]
                + [pl.BlockSpec((8, d), fixed)] * 3 + [VMEM_SPEC] * 3)
    f = lambda w: jax.ShapeDtypeStruct((total, w), BF16)
    out_shape = [f(HW), f(HW), f(HW), f(HW), f(d), f(d), jax.ShapeDtypeStruct((rows, d), BF16),
                 jax.ShapeDtypeStruct((rows, d), BF16), jax.ShapeDtypeStruct((rows, d), BF16),
                 jax.ShapeDtypeStruct((8, d), F32)]
    out_specs = ([pl.BlockSpec((TM, HW), full)] * 4 + [pl.BlockSpec((TM, d), full)] * 2
                 + [pl.BlockSpec((TM, d), lat)] * 3 + [pl.BlockSpec((8, d), fixed)])
    return pl.pallas_call(
        body, name=name, grid=(n_tiles,), out_shape=out_shape, in_specs=in_specs, out_specs=out_specs,
        compiler_params=_cparams(dimension_semantics=("arbitrary",)),
    )(x_lat, *o_list, p, p, p, p, p, p, dz2, y1, modx, norms, onorms, w_br_hg, w_br_gla, w_out)


def _in_projection_bwd(ctx0, x0, dz2, modc, modx, pre1, w_t, pieces, n_ctx_tiles, name):
    d = x0.shape[1]
    rows = ctx0.shape[0] + x0.shape[0]
    lat_rows = dz2.shape[0]
    width = P_WIDTH
    n_pieces = len(pieces)

    def body(*refs):
        ctx_ref, x_ref, dz2_ref, modc_ref, modx_ref, pre_ref, w_ref = refs[:7]
        (dhq_f, dhq_b, dhi_f, dhi_b, dhf_f, dhf_b, dhgate, dgq_f, dgq_b, dgk_f, dgk_b, dgv_f, dgv_b, dggate,
         dghg, dggla, dlr_f, dlr_b) = refs[7:7 + n_pieces]
        dp_ref, gx_ref, stat_ref = refs[7 + n_pieces:]
        i = pl.program_id(0)
        is_ctx = i < n_ctx_tiles
        z = jnp.where(is_ctx, ctx_ref[...], x_ref[...])
        sections = [
            (0, dhq_f[...] + dhq_b[...]), (HW, dhi_f[...] + dhi_b[...]), (2 * HW, dhf_f[...]), (3 * HW, dhf_b[...]),
            (4 * HW, dhgate[...]), (5 * HW, dgq_f[...] + dgq_b[...]), (6 * HW, dgk_f[...] + dgk_b[...]),
            (7 * HW, dgv_f[...] + dgv_b[...]), (8 * HW, dggate[...]),
            (9 * HW, dghg[:, 0:HW]), (10 * HW, dghg[:, HW:2 * HW]),
            (11 * HW, dggla[:, 0:HW]), (12 * HW, dggla[:, HW:2 * HW]), (OFF_LR, dlr_f[...] + dlr_b[...])]
        dh = jnp.zeros((TM, d), F32)
        for off, val in sections:
            w = val.shape[1]
            vb = val.astype(BF16)
            dp_ref[off:off + w, :] = vb.T
            dh = dh + jnp.dot(vb, w_ref[_w_in_row(off):_w_in_row(off) + w, :], preferred_element_type=F32)
        n, r = _rms(z)
        pre = pre_ref[...]
        scale = jnp.where(is_ctx, modc_ref[1:2, :], modx_ref[1:2, :])
        nw = n * pre
        dshift = _colsum(dh)
        dscale = _colsum(dh * nw)
        dnw = dh * (1.0 + scale)
        dpre = _colsum(dnw * n)
        gx_ref[...] = dz2_ref[...] + _rms_bwd(dnw * pre, n, r)
        zero = jnp.zeros((1, d), F32)

        @pl.when(i == 0)
        def _():
            stat_ref[...] = jnp.zeros_like(stat_ref)

        stat_ref[0:1, :] += jnp.where(is_ctx, zero, dshift)
        stat_ref[1:2, :] += jnp.where(is_ctx, zero, dscale)
        stat_ref[2:3, :] += jnp.where(is_ctx, dshift, zero)
        stat_ref[3:4, :] += jnp.where(is_ctx, dscale, zero)
        stat_ref[4:5, :] += dpre

    full = lambda i: (i, 0)
    lat = lambda i: (jnp.maximum(i - n_ctx_tiles, 0), 0)
    fixed = lambda i: (0, 0)
    piece_specs = [pl.BlockSpec((TM, a.shape[1]), full) for a in pieces]
    in_specs = [pl.BlockSpec((TM, d), lambda i: (jnp.minimum(i, n_ctx_tiles - 1), 0)), pl.BlockSpec((TM, d), lat),
                pl.BlockSpec((TM, d), lat), pl.BlockSpec((8, d), fixed),
                pl.BlockSpec((8, d), fixed), pl.BlockSpec((1, d), fixed), VMEM_SPEC] + piece_specs
    return pl.pallas_call(
        body, name=name, grid=(rows // TM,),
        out_shape=[jax.ShapeDtypeStruct((width, rows), BF16), jax.ShapeDtypeStruct((lat_rows, d), F32),
                   jax.ShapeDtypeStruct((8, d), F32)],
        in_specs=in_specs,
        out_specs=[pl.BlockSpec((width, TM), lambda i: (0, i)), pl.BlockSpec((TM, d), lat),
                   pl.BlockSpec((8, d), fixed)],
        compiler_params=_cparams(dimension_semantics=("arbitrary",)),
    )(ctx0, x0, dz2, modc, modx, pre1, w_t, *pieces)


def _transposed_lhs_matmul(x_ref, dy_ref, o_ref, xt_ref):
    @pl.when(pl.program_id(1) == 0)
    def _():
        xt_ref[...] = x_ref[...].T

    o_ref[...] = jnp.dot(xt_ref[...], dy_ref[...], preferred_element_type=F32)


def _w_in_grad(dp_t, h1, n_cols, name, after=()):
    rows, d = h1.shape
    n_main = OFF_LR // HW
    lr0 = _w_in_row(OFF_LR)

    def body(x_ref, xlr_ref, h_ref, *rest):
        o_hbm, acc_ref, sems = rest[len(after):]
        i = pl.program_id(0)
        slot = i % 2

        def main_copy(step):
            row = jnp.where(step < 9, step * HW, step * HW + 2 * RANK)
            return pltpu.make_async_copy(acc_ref.at[step % 2], o_hbm.at[pl.ds(pl.multiple_of(row, 8), HW), :],
                                         sems.at[step % 2])

        @pl.when(i > 1)
        def _():
            main_copy(i - 2).wait()

        @pl.when(i < n_main)
        def _():
            acc_ref[slot] = jnp.dot(x_ref[...], h_ref[...], preferred_element_type=F32)
            main_copy(i).start()

        @pl.when(i == n_main)
        def _():
            acc_ref[slot, 0:128, :] = jnp.dot(xlr_ref[...], h_ref[...], preferred_element_type=F32)
            lr_copy = pltpu.make_async_copy(acc_ref.at[slot, 0:2 * RANK, :], o_hbm.at[lr0:lr0 + 2 * RANK, :],
                                            sems.at[slot])
            lr_copy.start()
            main_copy(i - 1).wait()
            lr_copy.wait()

    return pl.pallas_call(
        body, name=name, grid=(n_main + 1,),
        out_shape=jax.ShapeDtypeStruct((n_cols, d), F32),
        in_specs=[pl.BlockSpec((HW, rows), lambda i: (jnp.minimum(i, n_main - 1), 0)),
                  pl.BlockSpec((128, rows), lambda i: (OFF_LR // 128, 0)),
                  pl.BlockSpec((rows, d), lambda i: (0, 0))] + [ANY_SPEC] * len(after),
        out_specs=ANY_SPEC,
        scratch_shapes=[pltpu.VMEM((2, HW, d), F32), pltpu.SemaphoreType.DMA((2,))],
        compiler_params=_cparams(dimension_semantics=("arbitrary",)),
    )(dp_t, dp_t, h1, *after)


def _weight_grad(xs, dy, name, tk=None, tn=512, k_first=0, k_tiles=None):
    rows = dy.shape[0]
    n = dy.shape[1]
    tn_ = min(tn, n)
    tk_ = xs.shape[1] if tk is None else tk
    k_tiles = xs.shape[1] // tk_ if k_tiles is None else k_tiles
    k = k_tiles * tk_

    return pl.pallas_call(
        functools.partial(_transposed_lhs_matmul), name=name, grid=(k_tiles, n // tn_),
        out_shape=jax.ShapeDtypeStruct((k, n), F32),
        in_specs=[pl.BlockSpec((rows, tk_), lambda i, j: (0, i + k_first)),
                  pl.BlockSpec((rows, tn_), lambda i, j: (0, j))],
        out_specs=pl.BlockSpec((tk_, tn_), lambda i, j: (i, j)),
        scratch_shapes=[pltpu.VMEM((tk_, rows), BF16)],
        compiler_params=_cparams(dimension_semantics=("parallel", "arbitrary")),
    )(xs, dy)


def _running_sums(xs, fws):
    c = xs[0].shape[0]
    row = lax.broadcasted_iota(jnp.int32, (c, 1), 0)
    s = 1
    while s < c:
        xs = [x + (jnp.where(row >= s, pltpu.roll(x, s, axis=0), 0.0) if fw else
                   jnp.where(row < c - s, pltpu.roll(x, c - s, axis=0), 0.0)) for x, fw in zip(xs, fws)]
        s *= 2
    return xs


def _chunks_terms(qs, ks, gs, fws):
    c = CHUNK
    n = len(qs)
    r = lax.broadcasted_iota(jnp.int32, (c, c), 0)
    s = lax.broadcasted_iota(jnp.int32, (c, c), 1)
    row = lax.broadcasted_iota(jnp.int32, (c, 1), 0)
    per_dir = {}
    for fw in set(fws):
        pos = row if fw else (c - 1 - row)
        per_dir[fw] = dict(
            causal=(s <= r) if fw else (s >= r), causal_t=(s >= r) if fw else (s <= r), pos=pos,
            in_blk=[(pos >= SUB * j) & (pos < SUB * (j + 1)) for j in range(NSUB)],
            start_row=[None] + [SUB * j - 1 if fw else c - SUB * j for j in range(1, NSUB)],
            rend=c - 1 if fw else 0)
    dirs = [per_dir[fw] for fw in fws]
    cums = _running_sums(gs, fws)
    starts = [[None] + [cum[d["start_row"][j]:d["start_row"][j] + 1, :] for j in range(1, NSUB)]
              for cum, d in zip(cums, dirs)]
    es = [[jnp.exp(cum) for cum in cums]]
    for j in range(1, NSUB):
        es.append([jnp.exp(jnp.where(d["pos"] >= SUB * j, cum - st[j], -1e30)) for cum, st, d in zip(cums, starts, dirs)])
    owns = [functools.reduce(lambda rest, j: jnp.where(d["in_blk"][j], st[j], rest), range(1, NSUB), 0.0)
            for st, d in zip(starts, dirs)]
    kscales = [jnp.exp(own - cum) for own, cum in zip(owns, cums)]
    cends = [cum[d["rend"]:d["rend"] + 1, :] for cum, d in zip(cums, dirs)]
    tails = [jnp.exp(cend - cum) for cend, cum in zip(cends, cums)]
    qcats = [jnp.concatenate([q * es[j][i] for j in range(NSUB)], axis=1).astype(BF16) for i, q in enumerate(qs)]
    kts = [k * ksc for k, ksc in zip(ks, kscales)]
    kms = [jnp.concatenate([jnp.where(d["in_blk"][j], kt, 0.0) for j in range(NSUB)], axis=1).astype(BF16)
           for kt, d in zip(kts, dirs)]
    e_by_lane = [[es[j][i] for j in range(NSUB)] for i in range(n)]
    return dict(dirs=dirs, e=e_by_lane, kscale=kscales, cend=cends, tail=tails, qcat=qcats, km=kms, kt=kts)


def _chunks_fwd(qs, ks, vs, gs, st0s, fws):
    t = _chunks_terms(qs, ks, gs, fws)
    scores = [_dot_nt(qc, km) for qc, km in zip(t["qcat"], t["km"])]
    a = [jnp.where(d["causal"], sc, 0.0) for sc, d in zip(scores, t["dirs"])]
    inter = [_dot_nt(qc[:, 0:HD], st0) for qc, st0 in zip(t["qcat"], st0s)]
    intra = [_dot(a_, v) for a_, v in zip(a, vs)]
    os_ = [x + y for x, y in zip(intra, inter)]
    upd = [_dot_tn(v, k * tl) for v, k, tl in zip(vs, ks, t["tail"])]
    st1s = [st0 * jnp.exp(ce) + u for st0, ce, u in zip(st0s, t["cend"], upd)]
    return os_, st1s


def _chunks_bwd(qs, ks, vs, gs, st0s, dos, dst1s, fws):
    n = len(qs)
    t = _chunks_terms(qs, ks, gs, fws)
    qcat, km, e, dirs = t["qcat"], t["km"], t["e"], t["dirs"]
    a_t = [jnp.where(d["causal_t"], _dot_nt(km_, qc), 0.0) for km_, qc, d in zip(km, qcat, dirs)]
    ktail = [k * tl for k, tl in zip(ks, t["tail"])]
    dv_a = [_dot(at, do) for at, do in zip(a_t, dos)]
    dv_b = [_dot_nt(kt, ds) for kt, ds in zip(ktail, dst1s)]
    dv = [x + y for x, y in zip(dv_a, dv_b)]
    da = [jnp.where(d["causal"], _dot_nt(do, v), 0.0) for do, v, d in zip(dos, vs, dirs)]
    da_t = [jnp.where(d["causal_t"], _dot_nt(v, do), 0.0) for do, v, d in zip(dos, vs, dirs)]
    dqcat = [_dot(da_, km_) for da_, km_ in zip(da, km)]
    dq_inter = [e[i][0] * _dot(dos[i], st0s[i]) for i in range(n)]
    dkm = [_dot(dat, qc) for dat, qc in zip(da_t, qcat)]
    dk_inter = [_dot(v, ds) * tl for v, ds, tl in zip(vs, dst1s, t["tail"])]
    dq = [dq_inter[i] + sum(e[i][j] * dqcat[i][:, j * HD:(j + 1) * HD] for j in range(NSUB)) for i in range(n)]
    dkt = [sum(jnp.where(dirs[i]["in_blk"][j], dkm[i][:, j * HD:(j + 1) * HD], 0.0) for j in range(NSUB))
           for i in range(n)]
    dk = [dkt[i] * t["kscale"][i] + dk_inter[i] for i in range(n)]
    dcum = [qs[i] * dq_inter[i] - ks[i] * dk_inter[i] - t["kt"][i].astype(BF16).astype(F32) * dkt[i]
            + sum(qcat[i][:, j * HD:(j + 1) * HD].astype(F32) * dqcat[i][:, j * HD:(j + 1) * HD] for j in range(NSUB))
            for i in range(n)]
    ecend = [jnp.exp(ce) for ce in t["cend"]]
    end = [ecend[i] * _colsum(st0s[i] * dst1s[i]) + _colsum(ks[i] * dk_inter[i]) for i in range(n)]
    sums = _running_sums(dcum, [not fw for fw in fws])
    dg = [sm + en for sm, en in zip(sums, end)]
    upd = [_dot_tn(dos[i], qs[i] * e[i][0]) for i in range(n)]
    dst0 = [dst1s[i] * ecend[i] + upd[i] for i in range(n)]
    return dq, dk, dv, dg, dst0


def _chunk_index(step, n_ctx_chunks, n_chunks, fw):
    if fw:
        return step
    return jnp.where(step < n_ctx_chunks, n_ctx_chunks - 1 - step, n_chunks - 1 + n_ctx_chunks - step)


def _hg_inputs(hq, hf, lbv, d_idx, sl):
    lb = _sigmoid(lbv[d_idx:d_idx + 1, sl] - lbv[2 + d_idx:3 + d_idx, sl])
    sg = _sigmoid(hf)
    f = lb + (1.0 - lb) * sg
    return _silu(hq), 1.0 - f, jnp.log(f), f, sg, lb


def _scan_fwd_both(p, branch_sides, n_ctx_chunks, name):
    rows = p.shape[0]
    n_chunks = rows // CHUNK
    n_ins = [4 if branch == "hg" else 6 for branch, _ in branch_sides]
    n_in_all = 2 * sum(n_ins)
    n_br = len(branch_sides)

    def body(*refs):
        ins, outs, state = refs[:n_in_all], refs[n_in_all:n_in_all + 4 * n_br], refs[-1]

        @pl.when(pl.program_id(0) == 0)
        def _():
            state[...] = jnp.zeros_like(state)

        lanes, where = [], []
        pos = 0
        for bi, (branch, _) in enumerate(branch_sides):
            hg = branch == "hg"
            n_in = n_ins[bi]
            for di, fw in enumerate((True, False)):
                r = ins[pos:pos + n_in]
                pos += n_in
                o_ref, st_ref = outs[4 * bi + 2 * di], outs[4 * bi + 2 * di + 1]
                if hg:
                    a_ref, b_ref, c_ref, lb_ref = r
                else:
                    a_ref, b_ref, c_ref, lr_ref, wgk_ref, bgk_ref = r
                    logits = _dot(lr_ref[...], wgk_ref[...]) + bgk_ref[...]
                    g_all = _log_sigmoid(logits) * (1.0 / GATE_NORM)
                for h in range(NH):
                    sl = slice(h * HD, (h + 1) * HD)
                    if hg:
                        q, k, g, _, _, _ = _hg_inputs(a_ref[:, sl], c_ref[:, sl], lb_ref[...], di, sl)
                        v = b_ref[:, sl]
                    else:
                        q, k, v, g = a_ref[:, sl] * (HD ** -0.5), b_ref[:, sl], c_ref[:, sl], g_all[:, sl]
                    lanes.append((q, k, v, g, state[2 * bi + di, h], fw))
                    where.append((2 * bi + di, h, sl, o_ref, st_ref))
        qs, ks, vs, gs, st0s, fws = (list(col) for col in zip(*lanes))
        os_, st1s = _chunks_fwd(qs, ks, vs, gs, st0s, fws)
        for (si, h, sl, o_ref, st_ref), st0, o, st1 in zip(where, st0s, os_, st1s):
            st_ref[0, h] = st0
            o_ref[:, sl] = o
            state[si, h] = st1

    fixed = lambda j: (0, 0)
    in_specs, args, out_specs = [], [], []
    for branch, side in branch_sides:
        for di, fw in enumerate((True, False)):
            chunk = functools.partial(_chunk_index, n_ctx_chunks=n_ctx_chunks, n_chunks=n_chunks, fw=fw)

            def cmap(blk, width=HW, chunk=chunk):
                return pl.BlockSpec((CHUNK, width), lambda j: (chunk(j), blk))

            if branch == "hg":
                in_specs += [cmap(C_HQ), cmap(C_HI), cmap(C_HF_FW + di), pl.BlockSpec((4, HW), fixed)]
                args += [p, p, p, side]
            else:
                in_specs += [cmap(C_GQ), cmap(C_GK), cmap(C_GV), cmap(OFF_LR // 128, 128),
                             pl.BlockSpec((128, HW), fixed), pl.BlockSpec((1, HW), fixed)]
                args += [p, p, p, p, side[di][0], side[di][1]]
            out_specs += [cmap(0), pl.BlockSpec((1, NH, HD, HD), lambda j, chunk=chunk: (chunk(j), 0, 0, 0))]
    return pl.pallas_call(
        body, name=name, grid=(n_chunks,),
        out_shape=[jax.ShapeDtypeStruct((rows, HW), F32),
                   jax.ShapeDtypeStruct((n_chunks, NH, HD, HD), F32)] * (2 * n_br),
        in_specs=in_specs, out_specs=out_specs,
        scratch_shapes=[pltpu.VMEM((2 * n_br, NH, HD, HD), F32)],
        compiler_params=_cparams(dimension_semantics=("arbitrary",)),
    )(*args)


def _scan_bwd_both(p, branch_items, n_ctx_chunks, name):
    rows = p.shape[0]
    n_chunks = rows // CHUNK
    n_ins = [6 if item[0] == "hg" else 8 for item in branch_items]
    n_outs = [4 if item[0] == "hg" else 6 for item in branch_items]
    n_in_all, n_out_all = 2 * sum(n_ins), 2 * sum(n_outs)

    def body(*refs):
        ins, outs, dstate = refs[:n_in_all], refs[n_in_all:n_in_all + n_out_all], refs[-1]
        first = pl.program_id(0) == 0

        @pl.when(first)
        def _():
            dstate[...] = jnp.zeros_like(dstate)

        lanes, where, extra, ctx = [], [], [], []
        ipos = opos = 0
        for bi, item in enumerate(branch_items):
            hg = item[0] == "hg"
            for di, fw in enumerate((True, False)):
                r, w = ins[ipos:ipos + n_ins[bi]], outs[opos:opos + n_outs[bi]]
                ipos += n_ins[bi]
                opos += n_outs[bi]
                if hg:
                    a_ref, b_ref, c_ref, lb_ref, st_ref, do_ref = r
                    acc_refs = (w[3],)
                else:
                    a_ref, b_ref, c_ref, lr_ref, wgk_ref, bgk_ref, st_ref, do_ref = r
                    acc_refs = (w[4], w[5])
                    lr = lr_ref[...]
                    logits = _dot(lr, wgk_ref[...]) + bgk_ref[...]
                    g_all = _log_sigmoid(logits) * (1.0 / GATE_NORM)

                @pl.when(first)
                def _(acc_refs=acc_refs):
                    for ref in acc_refs:
                        ref[...] = jnp.zeros_like(ref)

                for h in range(NH):
                    sl = slice(h * HD, (h + 1) * HD)
                    if hg:
                        hq, hf = a_ref[:, sl], c_ref[:, sl]
                        q, k, g, f, sg, lb = _hg_inputs(hq, hf, lb_ref[...], di, sl)
                        v = b_ref[:, sl]
                        extra.append((hq, f, sg, lb))
                    else:
                        q, k, v, g = a_ref[:, sl] * (HD ** -0.5), b_ref[:, sl], c_ref[:, sl], g_all[:, sl]
                        extra.append(None)
                    lanes.append((q, k, v, g, st_ref[0, h], do_ref[:, sl], dstate[2 * bi + di, h], fw))
                    where.append((2 * bi + di, h, sl))
                ctx.append((hg, w, None if hg else (lr, logits, wgk_ref)))

        dqs, dks, dvs, dgs, dst0s = [], [], [], [], []
        for lo in range(0, len(lanes), BWD_LANES):
            cols = [list(col) for col in zip(*lanes[lo:lo + BWD_LANES])]
            for acc, part in zip((dqs, dks, dvs, dgs, dst0s), _chunks_bwd(*cols)):
                acc.extend(part)
        dg_parts = [[] for _ in ctx]
        for (si, h, sl), ex, dq, dk, dv, dg, dst0 in zip(where, extra, dqs, dks, dvs, dgs, dst0s):
            dstate[si, h] = dst0
            hg, w, _ = ctx[si]
            if hg:
                hq, f, sg, lb = ex
                da_ref, db_ref, dc_ref, dlb_ref = w
                da_ref[:, sl] = (dq * _dsilu(hq)).astype(BF16)
                db_ref[:, sl] = dv.astype(BF16)
                df = dg / f - dk
                dc_ref[:, sl] = (df * (1.0 - lb) * sg * (1.0 - sg)).astype(BF16)
                dlb_ref[0:1, sl] += _colsum(df * (1.0 - sg))
            else:
                da_ref, db_ref, dc_ref = w[:3]
                da_ref[:, sl] = (dq * (HD ** -0.5)).astype(BF16)
                db_ref[:, sl] = dk.astype(BF16)
                dc_ref[:, sl] = dv.astype(BF16)
                dg_parts[si].append(dg)
        for si, (hg, w, more) in enumerate(ctx):
            if not hg:
                dlr_ref, dwgk_ref, dbias_ref = w[3:]
                lr, logits, wgk_ref = more
                dlogits = jnp.concatenate(dg_parts[si], axis=1) * (1.0 / GATE_NORM) * (1.0 - _sigmoid(logits))
                dlr_ref[...] = _dot_nt(dlogits, wgk_ref[...]).astype(BF16)
                dwgk_ref[...] += _dot_tn(lr, dlogits)
                dbias_ref[0:1, :] += _colsum(dlogits)

    fixed = lambda j: (0, 0)
    big = jax.ShapeDtypeStruct((rows, HW), BF16)
    in_specs, args, out_shape, out_specs = [], [], [], []
    for branch, side, states, d_o in branch_items:
        for di, fw in enumerate((True, False)):
            def chunk_of(j, fw=fw):
                return _chunk_index(n_chunks - 1 - j, n_ctx_chunks, n_chunks, fw)

            def cmap(blk, width=HW, chunk_of=chunk_of):
                return pl.BlockSpec((CHUNK, width), lambda j: (chunk_of(j), blk))

            st_spec = pl.BlockSpec((1, NH, HD, HD), lambda j, chunk_of=chunk_of: (chunk_of(j), 0, 0, 0))
            if branch == "hg":
                in_specs += [cmap(C_HQ), cmap(C_HI), cmap(C_HF_FW + di), pl.BlockSpec((4, HW), fixed), st_spec,
                             cmap(0)]
                args += [p, p, p, side, states[di], d_o]
                out_shape += [big, big, big, jax.ShapeDtypeStruct((8, HW), F32)]
                out_specs += [cmap(0), cmap(0), cmap(0), pl.BlockSpec((8, HW), fixed)]
            else:
                in_specs += [cmap(C_GQ), cmap(C_GK), cmap(C_GV), cmap(OFF_LR // 128, 128),
                             pl.BlockSpec((128, HW), fixed), pl.BlockSpec((1, HW), fixed), st_spec, cmap(0)]
                args += [p, p, p, p, side[di][0], side[di][1], states[di], d_o]
                out_shape += [big, big, big, jax.ShapeDtypeStruct((rows, 128), BF16),
                              jax.ShapeDtypeStruct((128, HW), F32), jax.ShapeDtypeStruct((8, HW), F32)]
                out_specs += [cmap(0), cmap(0), cmap(0), cmap(0, 128), pl.BlockSpec((128, HW), fixed),
                              pl.BlockSpec((8, HW), fixed)]
    return pl.pallas_call(
        body, name=name, grid=(n_chunks,), out_shape=out_shape, in_specs=in_specs, out_specs=out_specs,
        scratch_shapes=[pltpu.VMEM((2 * len(branch_items), NH, HD, HD), F32)],
        compiler_params=_cparams(dimension_semantics=("arbitrary",)),
    )(*args)


SMALL_ROWS = 56
ROWS_MOD_X = (0, 1, 8, 16, 17, 18)
ROWS_MOD_C = (2, 3)
ROW_PRE1, ROW_POST1, ROW_ONORM, ROW_PRE2, ROW_POST2, ROW_LB, ROW_BGK, ROW_WGK = 4, 9, 10, 19, 20, 24, 32, 40
ROW_LOSS = 21


def _reduce_small(gathered, lb_full, name, after=()):
    _, _, d = gathered.shape

    def body(g_ref, lb_ref, *rest):
        sum_ref, dmod_ref, dbmod_ref, dlb_ref = rest[len(after):]
        total = g_ref[0]
        for b in range(1, N_DEV):
            total = total + g_ref[b]
        sum_ref[...] = total
        dmod_ref[...] = jnp.zeros_like(dmod_ref)
        for m in range(N_MOD):
            col = slice(m * d, (m + 1) * d)
            acc = jnp.zeros((1, d), F32)
            for b in range(N_DEV):
                row = g_ref[b, ROWS_MOD_X[m]:ROWS_MOD_X[m] + 1, :]
                dmod_ref[b:b + 1, col] = row
                acc = acc + row
            if m < 2:
                ctx_row = total[ROWS_MOD_C[m]:ROWS_MOD_C[m] + 1, :]
                dmod_ref[8:9, col] = ctx_row
                acc = acc + ctx_row
            dbmod_ref[:, col] = acc
        lbv = lb_ref[...]
        for dd in range(2):
            lb = _sigmoid(lbv[dd:dd + 1, :] - lbv[2 + dd:3 + dd, :])
            gl = total[ROW_LB:ROW_LB + 1, dd * HW:(dd + 1) * HW] * lb * (1.0 - lb)
            dlb_ref[dd:dd + 1, :] = gl
            dlb_ref[2 + dd:3 + dd, :] = -gl

    return pl.pallas_call(
        body, name=name,
        out_shape=[jax.ShapeDtypeStruct((SMALL_ROWS, d), F32), jax.ShapeDtypeStruct((16, N_MOD * d), F32),
                   jax.ShapeDtypeStruct((1, N_MOD * d), F32), jax.ShapeDtypeStruct((4, HW), F32)],
        in_specs=[VMEM_SPEC] * 2 + [ANY_SPEC] * len(after), out_specs=[VMEM_SPEC] * 4, compiler_params=_cparams(),
    )(gathered, lb_full, *after)


def _c_ctx_grad(gathered, c_ctx_row, name):
    def body(g_ref, c_ref, o_ref):
        acc = g_ref[0, 0:1, :]
        for chip in range(1, N_CHIP):
            acc = acc + g_ref[2 * chip, 0:1, :]
        o_ref[...] = acc * _dsilu(c_ref[...])

    return pl.pallas_call(
        body, name=name, out_shape=jax.ShapeDtypeStruct(c_ctx_row.shape, F32),
        in_specs=[VMEM_SPEC] * 2, out_specs=VMEM_SPEC, compiler_params=_cparams(),
    )(gathered, c_ctx_row)


def _blocked(full, n_blocks):
    k, n = full.shape
    return full.reshape(k, n_blocks, n // n_blocks).transpose(1, 0, 2)


def _unblocked(blocks):
    nb, k, n = blocks.shape
    return blocks.transpose(1, 0, 2).reshape(k, nb * n)


def _sample_front(x0, ctx0, modc, modx, norm_pre1, lb_full, gla_side, w_in_r):
    ctx_len = ctx0.shape[0]
    n_ctx_tiles = ctx_len // TM
    n_ctx_chunks = ctx_len // CHUNK
    h1, p = _in_projection(ctx0, x0, modc, modx, norm_pre1, w_in_r, n_ctx_tiles, "in_projection")
    (o_hg_fw, st_hg_fw, o_hg_bw, st_hg_bw, o_gla_fw, st_gla_fw, o_gla_bw, st_gla_bw) = _scan_fwd_both(
        p, [("hg", lb_full), ("gla", gla_side)], n_ctx_chunks, "scan_fwd")
    return dict(h1=h1, p=p, o_list=[o_hg_fw, o_hg_bw, o_gla_fw, o_gla_bw],
                states=[st_hg_fw, st_hg_bw, st_gla_fw, st_gla_bw])


def _sample_back(reduce, front, x0, ctx0, target0, modc, modx, norm_pre1, norms, onorms, lb_full, gla_side, w_in_r,
                 wbh, wbg, wout, ffn_weights):
    seq, d = x0.shape
    ctx_len = ctx0.shape[0]
    n_ctx_tiles = ctx_len // TM
    n_tiles = (ctx_len + seq) // TM
    n_ctx_chunks = ctx_len // CHUNK
    h1, p, o_list = front["h1"], front["p"], front["o_list"]
    st_hg_fw, st_hg_bw, st_gla_fw, st_gla_bw = front["states"]
    z2, y1, merged, og_hg, og_gla = _mixer_tail_fwd(x0, p, o_list, modx, norms, onorms, wbh, wbg, wout, n_ctx_tiles,
                                                    "mixer_tail")
    wg, wu, wd = ffn_weights([z2])
    loss_part, dz2, h2, a_act, du, dv, dy2, stat_ffn = _ffn_fwd_bwd(z2, modx, norms, wg, wu, wd, target0, "ffn")
    dff = wg.shape[0]
    tok = reduce("ffn", [_weight_grad(du, h2, "grad_w_ff_gate", tk=dff // 2, tn=d),
                         _weight_grad(dv, h2, "grad_w_ff_up", tk=dff // 2, tn=d),
                         _weight_grad(a_act, dy2, "grad_w_ff_down", tk=dff // 2)])

    (d_ohg, d_ogla, d_hgate, d_ggate, d_ghg, d_ggla, dy1, db_hg, db_gla, stat_mix) = _mixer_tail_bwd(
        x0, p, o_list, dz2, y1, modx + tok, norms, onorms, wbh, wbg, wout, n_ctx_tiles, n_tiles, "mixer_tail_bwd")
    tok = reduce("mix", [_weight_grad(og_hg, db_hg, "grad_w_br_hg"), _weight_grad(og_gla, db_gla, "grad_w_br_gla"),
                         _weight_grad(merged, dy1, "grad_w_out")])
    tok = tok + reduce("push_ffn", [dy1])
    gla_b = [(wgk, bias + tok) for wgk, bias in gla_side]
    (dgq_f, dgk_f, dgv_f, dlr_f, dwgk_f, dbgk_f, dgq_b, dgk_b, dgv_b, dlr_b, dwgk_b, dbgk_b,
     dhq_f, dhi_f, dhf_f, dlb_f, dhq_b, dhi_b, dhf_b, dlb_b) = _scan_bwd_both(
        p, [("gla", gla_b, (st_gla_fw, st_gla_bw), d_ogla), ("hg", lb_full, (st_hg_fw, st_hg_bw), d_ohg)],
        n_ctx_chunks, "scan_bwd")
    tok = reduce("push_mix", [dbgk_f])
    pieces = [dhq_f, dhq_b, dhi_f, dhi_b, dhf_f, dhf_b, d_hgate, dgq_f, dgq_b, dgk_f, dgk_b, dgv_f, dgv_b, d_ggate,
              d_ghg, d_ggla, dlr_f, dlr_b]
    dp, grad_x, stat_in = _in_projection_bwd(ctx0, x0, dz2, modc, modx, norm_pre1 + tok, w_in_r, pieces, n_ctx_tiles,
                                             "in_projection_bwd")

    started = reduce("small_start", dict(stat_in=stat_in, stat_mix=stat_mix, stat_ffn=stat_ffn, dlb=(dlb_f, dlb_b),
                                         dwgk=(dwgk_f, dwgk_b), dbgk=(dbgk_f, dbgk_b)))
    reduce("in", [_w_in_grad(dp, h1, w_in_r.shape[0], "grad_w_in", after=[started])])
    reduce("small", None)
    reduce("push_in", [])
    return dict(loss_part=loss_part, grad_x=grad_x)


def kernel(x, c, ctx, c_ctx, w_mod, b_mod, norm_pre1, norm_post1, norm_pre2, norm_post2, w_in, hg_lb, hg_onorm, gla_w_gk, gla_b_gk, gla_onorm, w_br_hg, w_br_gla, w_out, w_ff_gate, w_ff_up, w_ff_down, loss_target, m_c_ctx, m_w_mod, m_b_mod, m_norm_pre1, m_norm_post1, m_norm_pre2, m_norm_post2, m_w_in, m_hg_lb, m_hg_onorm, m_gla_w_gk, m_gla_b_gk, m_gla_onorm, m_w_br_hg, m_w_br_gla, m_w_out, m_w_ff_gate, m_w_ff_up, m_w_ff_down, v_c_ctx, v_w_mod, v_b_mod, v_norm_pre1, v_norm_post1, v_norm_pre2, v_norm_post2, v_w_in, v_hg_lb, v_hg_onorm, v_gla_w_gk, v_gla_b_gk, v_gla_onorm, v_w_br_hg, v_w_br_gla, v_w_out, v_w_ff_gate, v_w_ff_up, v_w_ff_down):
    seq, d = x.shape[1], x.shape[2]
    ctx_len = ctx.shape[1]
    assert seq % TM == 0 and ctx_len % TM == 0 and d == 2 * HW
    ax, ay, ac = lax.axis_index("x"), lax.axis_index("y"), lax.axis_index("c")
    chip = 2 * ax + ay
    dev = 2 * chip + ac
    c_arr = jnp.reshape(ac, (1,)).astype(jnp.int32)
    chip_arr = jnp.reshape(chip, (1,)).astype(jnp.int32)
    transposed = ("w_in", "w_ff_gate", "w_ff_up")
    view = lambda a, nm: a[0].T if nm in transposed else a[0]

    sems_in, lands_in, token_in0 = _blocks_start([_cast_into_blocks(chip_arr, view(w_in, "w_in"), "cast_w_in")],
                                                 "gather_w_in_start")

    nc = d // 128
    pad8 = lambda a: jnp.pad(a, ((0, -a.shape[0] % 8), (0, 0)))
    small1 = jnp.concatenate([c.reshape(nc, 128) + token_in0[0, 0], pad8(hg_lb.reshape(4, 128)),
                              gla_w_gk.reshape(2 * RANK, 128), pad8(gla_b_gk.reshape(2, 128))], axis=0)
    blocks = [_cast_into_blocks(chip_arr, view(w_, nm), "cast_" + nm) for w_, nm in (
        (w_br_hg, "w_br_hg"), (w_br_gla, "w_br_gla"), (w_out, "w_out"), (w_ff_gate, "w_ff_gate"),
        (w_ff_up, "w_ff_up"), (w_ff_down, "w_ff_down"))]
    got1 = _allgather8(small1, "gather_small_params", after=blocks)
    c_all = got1[:, :nc, :].reshape(N_DEV, d)
    per_chip = got1[0::2]
    lb_full = per_chip[:, nc:nc + 4, :].transpose(1, 0, 2).reshape(4, HW)
    wgk_full = per_chip[:, nc + 8:nc + 8 + 2 * RANK, :].transpose(1, 0, 2).reshape(2, RANK, HW)
    bgk_full = per_chip[:, nc + 8 + 2 * RANK:nc + 10 + 2 * RANK, :].transpose(1, 0, 2).reshape(2, HW)
    wgk_pad = [jnp.zeros((128, HW), F32).at[dd * RANK:(dd + 1) * RANK].set(wgk_full[dd]) for dd in range(2)]
    bgk = [bgk_full[dd:dd + 1] for dd in range(2)]

    n_mod_cols = w_mod.shape[2]
    cond = jnp.concatenate([c_all, pad8(c_ctx.reshape(1, d))], axis=0)
    b_cols = lax.dynamic_slice(b_mod, (0, chip * n_mod_cols), (1, n_mod_cols))
    lands_in = _blocks_wait(sems_in, lands_in, [got1], "gather_w_in_wait")
    fwd_sems, lands_in, fwd_token = _forward_start(lands_in, "gather_w_in_forward_start")
    mod_part = _mod_forward(cond + fwd_token[0, 0], w_mod[0], b_cols, "mod_forward")
    mod_got = _allgather8(mod_part, "gather_mod")
    mod_all = mod_got[0::2].transpose(1, 0, 2).reshape(16, N_CHIP * n_mod_cols)
    modx = pad8(lax.dynamic_slice(mod_all, (dev, 0), (1, N_MOD * d)).reshape(N_MOD, d))
    modc = pad8(mod_all[8].reshape(N_MOD, d))

    gathered_in = _forward_wait(fwd_sems, lands_in, [mod_got], "gather_w_in_forward_wait")
    sems, lands, token = _blocks_start(blocks, "gather_rest_start", after=[gathered_in[0]])
    w_in_r = gathered_in[0].reshape(-1, d)

    norms = jnp.concatenate([norm_pre1, norm_post1, norm_pre2, norm_post2, jnp.zeros((4, d), F32)], axis=0)
    onorms = jnp.zeros((8, d), F32).at[0, :HD].set(hg_onorm[0]).at[1, :HD].set(gla_onorm[0])
    gla_side = [(wgk_pad[dd], bgk[dd]) for dd in range(2)]
    modx = modx + token[0, 0]
    front = _sample_front(x[0], ctx[0], modc, modx, norm_pre1, lb_full, gla_side, w_in_r)
    lands = _blocks_wait(sems, lands, front["o_list"], "gather_rest_wait")
    gathered = _blocks_finish(lands[:3], "gather_mix_finish")
    wbh, wbg = _unblocked(gathered[0]), _unblocked(gathered[1])
    wout = gathered[2].reshape(d, d)
    ffn_sems, ffn_lands, ffn_token = _forward_start(lands[3:], "gather_ffn_forward_start")
    onorms = onorms + ffn_token[0, 0]

    def ffn_weights(after):
        got = _forward_wait(ffn_sems, ffn_lands, after, "gather_ffn_forward_wait")
        return tuple(g.reshape(-1, d) for g in got)

    dff = w_ff_down.shape[1] * N_CHIP
    groups = {"ffn": ["w_ff_gate", "w_ff_up", "w_ff_down"], "mix": ["w_br_hg", "w_br_gla", "w_out"], "in": ["w_in"]}
    row_sharded = {"w_out": d // N_CHIP, "w_ff_down": dff // N_CHIP, "w_ff_gate": dff // N_CHIP,
                   "w_ff_up": dff // N_CHIP, "w_in": w_in.shape[2]}
    in_flight, to_sibling, small = {}, {}, {}

    def reduce_small_start(stats):
        small2 = jnp.concatenate([
            stats["stat_in"], stats["stat_mix"], stats["stat_ffn"],
            jnp.concatenate(stats["dlb"], axis=1), jnp.concatenate(stats["dbgk"], axis=1),
            jnp.concatenate([stats["dwgk"][0][0:RANK], stats["dwgk"][1][RANK:2 * RANK]], axis=1)], axis=0)
        assert small2.shape[0] == SMALL_ROWS
        sems_, land_, token_ = _allgather8_start(small2, "gather_small_grads_start")
        small.update(gathering=(sems_, land_))
        return token_

    def reduce_small():
        got2 = small["gathered"]
        total, dmod_all, g_b_mod, g_lb_full = _reduce_small(got2, lb_full, "reduce_small", to_sibling["in"][1])
        dmod_cols = lax.dynamic_slice(dmod_all, (0, chip * n_mod_cols), (16, n_mod_cols))
        g_w_mod, cctx_part = _mod_backward(cond, w_mod[0], dmod_cols, "mod_backward")
        got3 = _allgather8(cctx_part, "gather_c_ctx_grad")
        g_c_ctx = _c_ctx_grad(got3, c_ctx.reshape(1, d), "c_ctx_grad")
        small.update(total=total, g_b_mod=g_b_mod, g_lb_full=g_lb_full, g_w_mod=g_w_mod, g_c_ctx=g_c_ctx)

    def reduce(group, grads):
        if group == "small_start":
            return reduce_small_start(grads)
        if group == "small":
            return reduce_small()
        if group.startswith("push_"):
            return push(group[5:], grads)
        nms = groups[group]
        after = []
        if group == "in":
            small.update(gathered=_allgather8_wait(*small["gathering"], grads, "gather_small_grads_wait"))
            after = [small["gathered"]]
        full = [g.reshape(N_CHIP, row_sharded[nm], d) if nm in row_sharded else _blocked(g, N_CHIP)
                for g, nm in zip(grads, nms)]
        sems_, full, lands_, token_ = _send_half_start(full, "grads_to_sibling_start_" + group, after)
        to_sibling[group] = (sems_, full, lands_)
        return token_[0, 0]

    def push(group, after):
        nms = groups[group]
        sems_, full, lands_ = to_sibling[group]
        if group == "in":
            after = list(after) + [small["g_c_ctx"], small["total"]]
        full, from_sibling = _send_half_wait(sems_, full, lands_, after, "grads_to_sibling_wait_" + group)
        pairs = [_pair_sum(c_arr, f, r_, "pair_sum_" + nm) for f, r_, nm in zip(full, from_sibling, nms)]
        after = [small["g_c_ctx"], small["total"]] if group == "in" else []
        sems_, pairs, lands_, token_ = _scatter_start(pairs, "grads_to_owner_start_" + group, after)
        in_flight[group] = (sems_, pairs, lands_, token_)
        return token_[0, 0]

    r = _sample_back(reduce, front, x[0], ctx[0], loss_target[0], modc, modx, norm_pre1, norms, onorms, lb_full,
                     gla_side, w_in_r, wbh, wbg, wout, ffn_weights)
    grad_x = r["grad_x"]

    weights = dict(w_in=(w_in, m_w_in, v_w_in), w_br_hg=(w_br_hg, m_w_br_hg, v_w_br_hg),
                   w_br_gla=(w_br_gla, m_w_br_gla, v_w_br_gla), w_out=(w_out, m_w_out, v_w_out),
                   w_ff_gate=(w_ff_gate, m_w_ff_gate, v_w_ff_gate), w_ff_up=(w_ff_up, m_w_ff_up, v_w_ff_up),
                   w_ff_down=(w_ff_down, m_w_ff_down, v_w_ff_down))
    names = ["w_in", "w_br_hg", "w_br_gla", "w_out", "w_ff_gate", "w_ff_up", "w_ff_down"]
    big, swapping = {}, {}

    def sum_and_swap(group, after):
        sems_, pairs, lands_, _ = in_flight[group]
        pairs, lands_ = _scatter_wait(sems_, pairs, lands_, after, "grads_to_owner_wait_" + group)
        own_half = [_sum_owner(chip_arr, pr, g, "chip_sum_" + nm) for pr, g, nm in zip(pairs, lands_, groups[group])]
        swapping[group] = _swap_start(own_half, "halves_to_sibling_start_" + group)
        return own_half[-1]

    def update(group, after):
        sems_, own_half, lands_ = swapping[group]
        own_half, other_half = _swap_wait(sems_, own_half, lands_, after, "halves_to_sibling_wait_" + group)
        done = []
        for nm, own, oth in zip(groups[group], own_half, other_half):
            w_, m_, v_ = (view(a, nm) for a in weights[nm])
            res = _adamw_halves(c_arr, own, oth, w_, m_, v_, "adamw_" + nm)
            big[nm] = [r_.T[None] if nm in transposed else r_[None] for r_ in res]
            done.append(res[1])
        return done

    token_in = in_flight["in"][3]
    summed_ffn = sum_and_swap("ffn", [token_in])
    summed_mix = sum_and_swap("mix", [summed_ffn])

    total, g_b_mod, g_lb_full, g_w_mod, g_c_ctx = (small[k] for k in ("total", "g_b_mod", "g_lb_full", "g_w_mod",
                                                                      "g_c_ctx"))
    g_pre1, g_post1, g_pre2, g_post2 = (total[r_:r_ + 1] for r_ in (ROW_PRE1, ROW_POST1, ROW_PRE2, ROW_POST2))
    g_hg_on, g_gla_on = total[ROW_ONORM:ROW_ONORM + 1, 0:HD], total[ROW_ONORM:ROW_ONORM + 1, HD:2 * HD]
    n_lb = hg_lb.shape[2]
    g_hg_lb = lax.dynamic_slice(g_lb_full, (0, chip * n_lb), (4, n_lb))
    g_bgk = lax.dynamic_slice(total[ROW_BGK:ROW_BGK + 1].reshape(2, HW), (0, chip * n_lb), (2, n_lb))
    g_wgk_full = total[ROW_WGK:ROW_WGK + RANK].reshape(RANK, 2, HW).transpose(1, 0, 2).reshape(2 * RANK, HW)
    g_wgk = lax.dynamic_slice(g_wgk_full, (0, chip * n_lb), (2 * RANK, n_lb))

    small_items = [
        (g_c_ctx, c_ctx.reshape(1, d), m_c_ctx.reshape(1, d), v_c_ctx.reshape(1, d)),
        (g_b_mod, b_mod, m_b_mod, v_b_mod),
        (g_pre1, norm_pre1, m_norm_pre1, v_norm_pre1),
        (g_post1, norm_post1, m_norm_post1, v_norm_post1),
        (g_pre2, norm_pre2, m_norm_pre2, v_norm_pre2),
        (g_post2, norm_post2, m_norm_post2, v_norm_post2),
        (g_hg_lb, hg_lb.reshape(4, n_lb), m_hg_lb.reshape(4, n_lb), v_hg_lb.reshape(4, n_lb)),
        (g_hg_on, hg_onorm, m_hg_onorm, v_hg_onorm),
        (g_wgk, gla_w_gk.reshape(2 * RANK, n_lb), m_gla_w_gk.reshape(2 * RANK, n_lb), v_gla_w_gk.reshape(2 * RANK, n_lb)),
        (g_bgk, gla_b_gk.reshape(2, n_lb), m_gla_b_gk.reshape(2, n_lb), v_gla_b_gk.reshape(2, n_lb)),
        (g_gla_on, gla_onorm, m_gla_onorm, v_gla_onorm),
    ]
    small_res = _adamw_whole(small_items, "adamw_small")
    mod_res = _adamw_tiled(g_w_mod, w_mod[0], m_w_mod[0], v_w_mod[0], "adamw_w_mod")
    done_ffn = update("ffn", [summed_mix, mod_res[0], small_res[0][0]])
    done_mix = update("mix", done_ffn)
    update("in", [sum_and_swap("in", done_mix)])

    loss = total[ROW_LOSS, 0]

    shapes = dict(c_ctx=c_ctx.shape, b_mod=b_mod.shape, norm_pre1=norm_pre1.shape, norm_post1=norm_post1.shape,
                  norm_pre2=norm_pre2.shape, norm_post2=norm_post2.shape, hg_lb=hg_lb.shape, hg_onorm=hg_onorm.shape,
                  gla_w_gk=gla_w_gk.shape, gla_b_gk=gla_b_gk.shape, gla_onorm=gla_onorm.shape)
    small_names = ["c_ctx", "b_mod", "norm_pre1", "norm_post1", "norm_pre2", "norm_post2", "hg_lb", "hg_onorm",
                   "gla_w_gk", "gla_b_gk", "gla_onorm"]
    grads, deltas, new_m, new_v = {}, {}, {}, {}
    for nm, item, res in zip(small_names, small_items, small_res):
        grads[nm] = item[0].reshape(shapes[nm])
        deltas[nm], new_m[nm], new_v[nm] = (r_.reshape(shapes[nm]) for r_ in res)
    grads["w_mod"] = g_w_mod[None]
    deltas["w_mod"], new_m["w_mod"], new_v["w_mod"] = (r_[None] for r_ in mod_res)
    for nm in names:
        grads[nm], deltas[nm], new_m[nm], new_v[nm] = big[nm]
    order = ["c_ctx", "w_mod", "b_mod", "norm_pre1", "norm_post1", "norm_pre2", "norm_post2", "w_in", "hg_lb",
             "hg_onorm", "gla_w_gk", "gla_b_gk", "gla_onorm", "w_br_hg", "w_br_gla", "w_out", "w_ff_gate", "w_ff_up",
             "w_ff_down"]
    return (loss, grad_x[None], *[grads[n] for n in order], *[deltas[n] for n in order],
            *[new_m[n] for n in order], *[new_v[n] for n in order])
```

```python
import functools

import jax
import jax.numpy as jnp
from jax import lax
from jax.experimental import pallas as pl
from jax.experimental.pallas import tpu as pltpu

F32 = jnp.float32
BF16 = jnp.bfloat16
MESH = pl.DeviceIdType.MESH

EPS = 1e-6
CHUNK = 64
SUB = 16
NSUB = CHUNK // SUB
NH = 4
HD = 128
HW = NH * HD
RANK = 16
GATE_NORM = 16.0
N_MOD = 6
TM = 256
BWD_LANES = 8
N_DEV = 8
N_CHIP = 4
VMEM_LIMIT = 56 * 1024 * 1024

ADAM_LR = 0.001
ADAM_B1 = 0.9
ADAM_B2 = 0.999
ADAM_EPS = 1e-08
ADAM_WD = 0.01
ADAM_STEP = 10

VMEM_SPEC = pl.BlockSpec(memory_space=pltpu.VMEM)
ANY_SPEC = pl.BlockSpec(memory_space=pl.ANY)
HBM_SPEC = pl.BlockSpec(memory_space=pltpu.HBM)
SEM_SPEC = pl.BlockSpec(memory_space=pltpu.SEMAPHORE)
EFFECT = pltpu.SideEffectType.DATAFLOW_SIDE_EFFECTING


def _cparams(**kw):
    return pltpu.CompilerParams(vmem_limit_bytes=VMEM_LIMIT, **kw)


def _dot(a, b):
    return jnp.dot(a.astype(BF16), b.astype(BF16), preferred_element_type=F32)


def _dot_nt(a, b):
    return lax.dot_general(a.astype(BF16), b.astype(BF16), (((1,), (1,)), ((), ())), preferred_element_type=F32)


def _dot_tn(a, b):
    return lax.dot_general(a.astype(BF16), b.astype(BF16), (((0,), (0,)), ((), ())), preferred_element_type=F32)


def _sigmoid(x):
    return 1.0 / (1.0 + jnp.exp(-x))


def _silu(x):
    return x * _sigmoid(x)


def _dsilu(x):
    s = _sigmoid(x)
    return s * (1.0 + x * (1.0 - s))


def _log_sigmoid(x):
    return jnp.minimum(x, 0.0) - jnp.log(1.0 + jnp.exp(-jnp.abs(x)))


def _colsum(a):
    return jnp.sum(a, axis=0, keepdims=True)


def _rms(a):
    r = lax.rsqrt(jnp.mean(a * a, axis=-1, keepdims=True) + EPS)
    return a * r, r


def _rms_bwd(dn, n, r):
    return r * (dn - n * jnp.mean(dn * n, axis=-1, keepdims=True))


def _place():
    x, y, c = lax.axis_index("x"), lax.axis_index("y"), lax.axis_index("c")
    chips = [(1 - x, y), (x, 1 - y), (1 - x, 1 - y)]
    return x, y, c, chips


def _allgather8(v, name, after=()):
    rows, cols = v.shape
    n_after = len(after)

    def body(x_ref, *rest):
        out_ref, send_sems, recv_sems, local_sem = rest[n_after:]
        x, y, c, chips = _place()
        me, sibling = (x, y, c), (x, y, 1 - c)

        def blk(px, py, pc):
            return out_ref.at[4 * px + 2 * py + pc]

        def copy(k, block, to, src=None):
            return pltpu.make_async_remote_copy(
                src_ref=blk(*block) if src is None else src, dst_ref=blk(*block),
                send_sem=send_sems.at[k], recv_sem=recv_sems.at[k], device_id=to, device_id_type=MESH)

        mine = pltpu.make_async_copy(x_ref, blk(*me), local_sem)
        mine.start()
        first = [copy(0, me, sibling, src=x_ref)]
        first += [copy(1 + j, me, (*chip, c), src=x_ref) for j, chip in enumerate(chips)]
        for cp in first:
            cp.start()
        passed = [copy(4 + j, (*chip, c), sibling) for j, chip in enumerate(chips)]
        for j, chip in enumerate(chips):
            copy(1 + j, (*chip, c), me).wait_recv()
            passed[j].start()
        copy(0, sibling, me).wait_recv()
        for j, chip in enumerate(chips):
            copy(4 + j, (*chip, 1 - c), me).wait_recv()
        for cp in first + passed:
            cp.wait_send()
        mine.wait()

    return pl.pallas_call(
        body, name=name,
        out_shape=jax.ShapeDtypeStruct((N_DEV, rows, cols), v.dtype),
        in_specs=[VMEM_SPEC] + [ANY_SPEC] * n_after, out_specs=VMEM_SPEC,
        scratch_shapes=[pltpu.SemaphoreType.DMA((7,)), pltpu.SemaphoreType.DMA((7,)), pltpu.SemaphoreType.DMA],
    )(v, *after)


def _allgather8_start(v, name):
    rows, cols = v.shape

    def body(x_ref, out_ref, *rest):
        send_sems, recv_sems, token, local_sem = rest[:4], rest[4:8], rest[-2], rest[-1]
        x, y, c, chips = _place()
        own = out_ref.at[4 * x + 2 * y + c]
        mine = pltpu.make_async_copy(x_ref, own, local_sem)
        mine.start()
        mine.wait()
        for k, to in enumerate([(x, y, 1 - c)] + [(*chip, c) for chip in chips]):
            pltpu.make_async_remote_copy(src_ref=own, dst_ref=own, send_sem=send_sems[k], recv_sem=recv_sems[k],
                                         device_id=to, device_id_type=MESH).start()
        token[...] = jnp.zeros_like(token)

    land = _hbm(lax.empty((N_DEV, rows, cols), v.dtype))
    out = pl.pallas_call(
        body, name=name,
        out_shape=(*[pltpu.SemaphoreType.DMA(())] * 8, pltpu.HBM(land.shape, land.dtype),
                   jax.ShapeDtypeStruct((8, 128), F32)),
        in_specs=[VMEM_SPEC, HBM_SPEC],
        out_specs=(*[SEM_SPEC] * 8, HBM_SPEC, VMEM_SPEC),
        input_output_aliases={1: 8},
        scratch_shapes=[pltpu.SemaphoreType.DMA],
        compiler_params=pltpu.CompilerParams(has_side_effects=EFFECT),
    )(v, land)
    return list(out[:8]), out[8], out[9]


def _allgather8_wait(sems, land, after, name):
    def blk(ref, px, py, pc):
        return ref.at[4 * px + 2 * py + pc]

    def wait_body(out_ref, *rest):
        send_sems, recv_sems = rest[:4], rest[4:8]
        x, y, c, chips = _place()
        me = (x, y, c)
        for k, peer in enumerate([(x, y, 1 - c)] + [(*chip, c) for chip in chips]):
            sent = pltpu.make_async_remote_copy(
                src_ref=blk(out_ref, *me), dst_ref=blk(out_ref, *me), send_sem=send_sems[k], recv_sem=recv_sems[k],
                device_id=peer, device_id_type=MESH)
            sent.wait_send()
            pltpu.make_async_remote_copy(
                src_ref=blk(out_ref, *peer), dst_ref=blk(out_ref, *peer), send_sem=send_sems[k],
                recv_sem=recv_sems[k], device_id=me, device_id_type=MESH).wait_recv()

    def pass_body(out_ref, _, send_sems, recv_sems):
        x, y, c, chips = _place()
        started = []
        for j, chip in enumerate(chips):
            cp = pltpu.make_async_remote_copy(
                src_ref=blk(out_ref, *chip, c), dst_ref=blk(out_ref, *chip, c), send_sem=send_sems.at[j],
                recv_sem=recv_sems.at[j], device_id=(x, y, 1 - c), device_id_type=MESH)
            cp.start()
            started.append(cp)
        for j, chip in enumerate(chips):
            pltpu.make_async_remote_copy(
                src_ref=blk(out_ref, *chip, 1 - c), dst_ref=blk(out_ref, *chip, 1 - c), send_sem=send_sems.at[j],
                recv_sem=recv_sems.at[j], device_id=(x, y, c), device_id_type=MESH).wait_recv()
        for cp in started:
            cp.wait_send()

    land = pl.pallas_call(
        wait_body, name=name,
        out_shape=pltpu.HBM(land.shape, land.dtype),
        in_specs=[HBM_SPEC] + [SEM_SPEC] * 8 + [ANY_SPEC] * len(after),
        out_specs=HBM_SPEC,
        input_output_aliases={0: 0},
        compiler_params=pltpu.CompilerParams(has_side_effects=EFFECT),
    )(land, *sems, *after)
    return pl.pallas_call(
        pass_body, name=name + "_pass",
        out_shape=jax.ShapeDtypeStruct(land.shape, land.dtype),
        in_specs=[ANY_SPEC], out_specs=ANY_SPEC,
        input_output_aliases={0: 0},
        scratch_shapes=[pltpu.SemaphoreType.DMA((3,)), pltpu.SemaphoreType.DMA((3,))],
    )(land)


def _cast_into_blocks(chip_arr, w, name):
    rows, cols = w.shape
    tr = _row_tile(rows, 16, 256)

    def body(chip_ref, w_ref, o_ref):
        o_ref[0] = w_ref[...].astype(BF16)

    return pl.pallas_call(
        body, name=name,
        grid_spec=pltpu.PrefetchScalarGridSpec(
            num_scalar_prefetch=1, grid=(rows // tr,),
            in_specs=[pl.BlockSpec((tr, cols), lambda i, chip_ref: (i, 0))],
            out_specs=pl.BlockSpec((1, tr, cols), lambda i, chip_ref: (chip_ref[0], i, 0))),
        out_shape=jax.ShapeDtypeStruct((N_CHIP, rows, cols), BF16),
        compiler_params=_cparams(dimension_semantics=("parallel",)),
    )(chip_arr, w)


def _halved_by_rows(shape):
    return (shape[1] // 2) % 16 == 0


def _half_of(ref, pc, block=None):
    lead = slice(None) if block is None else block
    if _halved_by_rows(ref.shape):
        h = ref.shape[1] // 2
        return ref.at[lead, pl.ds(pl.multiple_of(pc * h, 16), h), :]
    h = ref.shape[2] // 2
    return ref.at[lead, :, pl.ds(pl.multiple_of(pc * h, 128), h)]


def _half_shape(shape):
    return (shape[0], shape[1] // 2, shape[2]) if _halved_by_rows(shape) else (shape[0], shape[1], shape[2] // 2)


def _half_rows(ref, chip_id, pc):
    return _half_of(ref, pc, chip_id)


def _hbm(a):
    return pltpu.with_memory_space_constraint(a, pltpu.HBM)


def _blocks_start(lands, name, after=()):
    n = len(lands)
    n_sem = 3 * n
    first = n + len(after)

    def body(*refs):
        lnd = refs[:n]
        send_sems, recv_sems = refs[first:first + n_sem], refs[first + n_sem:first + 2 * n_sem]
        token = refs[-1]
        x, y, c, chips = _place()
        me_chip = 2 * x + y
        for k in range(n):
            for j, chip in enumerate(chips):
                pltpu.make_async_remote_copy(
                    src_ref=_half_rows(lnd[k], me_chip, c), dst_ref=_half_rows(lnd[k], me_chip, c),
                    send_sem=send_sems[3 * k + j], recv_sem=recv_sems[3 * k + j],
                    device_id=(*chip, c), device_id_type=MESH).start()
        token[...] = jnp.zeros_like(token)

    out = pl.pallas_call(
        body, name=name,
        out_shape=(*[pltpu.SemaphoreType.DMA(())] * (2 * n_sem),
                   *[pltpu.HBM(l.shape, l.dtype) for l in lands],
                   jax.ShapeDtypeStruct((8, 128), F32)),
        in_specs=[HBM_SPEC] * n + [ANY_SPEC] * len(after),
        out_specs=(*[SEM_SPEC] * (2 * n_sem), *[HBM_SPEC] * n, VMEM_SPEC),
        input_output_aliases={i: 2 * n_sem + i for i in range(n)},
        compiler_params=pltpu.CompilerParams(has_side_effects=EFFECT),
    )(*[_hbm(l) for l in lands], *after)
    return list(out[:2 * n_sem]), list(out[2 * n_sem:2 * n_sem + n]), out[-1]


def _blocks_wait(sems, lands, after, name):
    n = len(lands)
    n_sem = 3 * n

    def body(*refs):
        lnd = refs[:n]
        s_sems, r_sems = refs[n:n + n_sem], refs[n + n_sem:n + 2 * n_sem]
        x, y, c, chips = _place()
        me_chip = 2 * x + y
        for k in range(n):
            for j, (px, py) in enumerate(chips):
                cp = pltpu.make_async_remote_copy(
                    src_ref=_half_rows(lnd[k], me_chip, c), dst_ref=_half_rows(lnd[k], 2 * px + py, c),
                    send_sem=s_sems[3 * k + j], recv_sem=r_sems[3 * k + j],
                    device_id=(px, py, c), device_id_type=MESH)
                cp.wait_send()
                cp.wait_recv()

    out = pl.pallas_call(
        body, name=name,
        out_shape=tuple(pltpu.HBM(l.shape, l.dtype) for l in lands),
        in_specs=[HBM_SPEC] * n + [SEM_SPEC] * (2 * n_sem) + [ANY_SPEC] * len(after),
        out_specs=[HBM_SPEC] * n,
        input_output_aliases={i: i for i in range(n)},
        compiler_params=pltpu.CompilerParams(has_side_effects=EFFECT),
    )(*lands, *sems, *after)
    return list(out)


def _forward_start(lands, name):
    n = len(lands)
    n_sem = 3 * n

    def body(*refs):
        lnd = refs[:n]
        send_sems, recv_sems = refs[n:n + n_sem], refs[n + n_sem:n + 2 * n_sem]
        x, y, c, chips = _place()
        for k in range(n):
            for j, (px, py) in enumerate(chips):
                pltpu.make_async_remote_copy(
                    src_ref=_half_rows(lnd[k], 2 * px + py, c), dst_ref=_half_rows(lnd[k], 2 * px + py, c),
                    send_sem=send_sems[3 * k + j], recv_sem=recv_sems[3 * k + j],
                    device_id=(x, y, 1 - c), device_id_type=MESH).start()
        refs[-1][...] = jnp.zeros_like(refs[-1])

    out = pl.pallas_call(
        body, name=name,
        out_shape=(*[pltpu.SemaphoreType.DMA(())] * (2 * n_sem), *[pltpu.HBM(l.shape, l.dtype) for l in lands],
                   jax.ShapeDtypeStruct((8, 128), F32)),
        in_specs=[HBM_SPEC] * n,
        out_specs=(*[SEM_SPEC] * (2 * n_sem), *[HBM_SPEC] * n, VMEM_SPEC),
        input_output_aliases={i: 2 * n_sem + i for i in range(n)},
        compiler_params=pltpu.CompilerParams(has_side_effects=EFFECT),
    )(*[_hbm(l) for l in lands])
    return list(out[:2 * n_sem]), list(out[2 * n_sem:2 * n_sem + n]), out[-1]


def _forward_wait(sems, lands, after, name):
    n = len(lands)
    n_sem = 3 * n

    def body(*refs):
        lnd = refs[:n]
        s_sems, r_sems = refs[n:n + n_sem], refs[n + n_sem:n + 2 * n_sem]
        x, y, c, chips = _place()
        for k in range(n):
            for j, (px, py) in enumerate(chips):
                cp = pltpu.make_async_remote_copy(
                    src_ref=_half_rows(lnd[k], 2 * px + py, c), dst_ref=_half_rows(lnd[k], 2 * px + py, 1 - c),
                    send_sem=s_sems[3 * k + j], recv_sem=r_sems[3 * k + j],
                    device_id=(x, y, 1 - c), device_id_type=MESH)
                cp.wait_send()
                cp.wait_recv()

    out = pl.pallas_call(
        body, name=name,
        out_shape=tuple(pltpu.HBM(l.shape, l.dtype) for l in lands),
        in_specs=[HBM_SPEC] * n + [SEM_SPEC] * (2 * n_sem) + [ANY_SPEC] * len(after),
        out_specs=[HBM_SPEC] * n,
        input_output_aliases={i: i for i in range(n)},
        compiler_params=pltpu.CompilerParams(has_side_effects=EFFECT),
    )(*lands, *sems, *after)
    return list(out)


def _blocks_finish(lands, name):
    n = len(lands)

    def body(*refs):
        lnd = refs[n:2 * n]
        send_sems, recv_sems = refs[2 * n:]
        x, y, c, chips = _place()
        sibling = (x, y, 1 - c)

        def copy(k, j, chip_id, pc):
            return pltpu.make_async_remote_copy(
                src_ref=_half_rows(lnd[k], chip_id, pc), dst_ref=_half_rows(lnd[k], chip_id, pc),
                send_sem=send_sems.at[k, j], recv_sem=recv_sems.at[k, j], device_id=sibling, device_id_type=MESH)

        started = []
        for k in range(n):
            for j, (px, py) in enumerate(chips):
                cp = copy(k, j, 2 * px + py, c)
                cp.start()
                started.append(cp)
        for k in range(n):
            for j, (px, py) in enumerate(chips):
                copy(k, j, 2 * px + py, 1 - c).wait_recv()
        for cp in started:
            cp.wait_send()

    out = pl.pallas_call(
        body, name=name,
        out_shape=[jax.ShapeDtypeStruct(l.shape, l.dtype) for l in lands],
        in_specs=[ANY_SPEC] * n, out_specs=[ANY_SPEC] * n,
        input_output_aliases={i: i for i in range(n)},
        scratch_shapes=[pltpu.SemaphoreType.DMA((n, 3)), pltpu.SemaphoreType.DMA((n, 3))],
    )(*lands)
    return list(out)


def _send_half_start(arrs, name, after=()):
    n = len(arrs)
    first = 2 * n + len(after)

    def body(*refs):
        ins, lnd = refs[:n], refs[n:2 * n]
        send_sems, recv_sems = refs[first:first + n], refs[first + n:first + 2 * n]
        token = refs[-1]
        x, y, c, _ = _place()
        for k in range(n):
            pltpu.make_async_remote_copy(
                src_ref=_half_of(ins[k], 1 - c), dst_ref=lnd[k], send_sem=send_sems[k], recv_sem=recv_sems[k],
                device_id=(x, y, 1 - c), device_id_type=MESH).start()
        token[...] = jnp.zeros_like(token)

    lands = [_hbm(lax.empty(_half_shape(a.shape), a.dtype)) for a in arrs]
    out = pl.pallas_call(
        body, name=name,
        out_shape=(*[pltpu.SemaphoreType.DMA(())] * (2 * n), *[pltpu.HBM(a.shape, a.dtype) for a in arrs],
                   *[pltpu.HBM(l.shape, l.dtype) for l in lands], jax.ShapeDtypeStruct((8, 128), F32)),
        in_specs=[HBM_SPEC] * (2 * n) + [ANY_SPEC] * len(after),
        out_specs=(*[SEM_SPEC] * (2 * n), *[HBM_SPEC] * (2 * n), VMEM_SPEC),
        input_output_aliases={i: 2 * n + i for i in range(2 * n)},
        compiler_params=pltpu.CompilerParams(has_side_effects=EFFECT),
    )(*[_hbm(a) for a in arrs], *lands, *after)
    return list(out[:2 * n]), list(out[2 * n:3 * n]), list(out[3 * n:4 * n]), out[-1]


def _send_half_wait(sems, arrs, lands, after, name):
    n = len(arrs)

    def body(*refs):
        ins, lnd = refs[:n], refs[n:2 * n]
        s_sems, r_sems = refs[2 * n:3 * n], refs[3 * n:4 * n]
        x, y, c, _ = _place()
        for k in range(n):
            cp = pltpu.make_async_remote_copy(
                src_ref=_half_of(ins[k], 1 - c), dst_ref=lnd[k], send_sem=s_sems[k], recv_sem=r_sems[k],
                device_id=(x, y, 1 - c), device_id_type=MESH)
            cp.wait_send()
            cp.wait_recv()

    out = pl.pallas_call(
        body, name=name,
        out_shape=tuple(pltpu.HBM(a.shape, a.dtype) for a in list(arrs) + list(lands)),
        in_specs=[HBM_SPEC] * (2 * n) + [SEM_SPEC] * (2 * n) + [ANY_SPEC] * len(after),
        out_specs=[HBM_SPEC] * (2 * n),
        input_output_aliases={i: i for i in range(2 * n)},
        compiler_params=pltpu.CompilerParams(has_side_effects=EFFECT),
    )(*arrs, *lands, *sems, *after)
    return list(out[:n]), list(out[n:])


def _scatter_start(arrs, name, after=()):
    n = len(arrs)
    n_sem = 3 * n
    first = 2 * n + len(after)

    def body(*refs):
        ins, lnd = refs[:n], refs[n:2 * n]
        send_sems, recv_sems = refs[first:first + n_sem], refs[first + n_sem:first + 2 * n_sem]
        token = refs[-1]
        x, y, c, chips = _place()
        me_chip = 2 * x + y
        for k in range(n):
            for j, (px, py) in enumerate(chips):
                pltpu.make_async_remote_copy(
                    src_ref=ins[k].at[2 * px + py], dst_ref=lnd[k].at[me_chip],
                    send_sem=send_sems[3 * k + j], recv_sem=recv_sems[3 * k + j],
                    device_id=(px, py, c), device_id_type=MESH).start()
        token[...] = jnp.zeros_like(token)

    lands = [_hbm(lax.empty(a.shape, a.dtype)) for a in arrs]
    out = pl.pallas_call(
        body, name=name,
        out_shape=(*[pltpu.SemaphoreType.DMA(())] * (2 * n_sem),
                   *[pltpu.HBM(a.shape, a.dtype) for a in arrs], *[pltpu.HBM(a.shape, a.dtype) for a in arrs],
                   jax.ShapeDtypeStruct((8, 128), F32)),
        in_specs=[HBM_SPEC] * (2 * n) + [ANY_SPEC] * len(after),
        out_specs=(*[SEM_SPEC] * (2 * n_sem), *[HBM_SPEC] * (2 * n), VMEM_SPEC),
        input_output_aliases={i: 2 * n_sem + i for i in range(2 * n)},
        compiler_params=pltpu.CompilerParams(has_side_effects=EFFECT),
    )(*[_hbm(a) for a in arrs], *lands, *after)
    base = 2 * n_sem
    return list(out[:base]), list(out[base:base + n]), list(out[base + n:base + 2 * n]), out[-1]


def _scatter_wait(sems, arrs, lands, after, name):
    n = len(arrs)
    n_sem = 3 * n

    def body(*refs):
        ins, lnd = refs[:n], refs[n:2 * n]
        s_sems, r_sems = refs[2 * n:2 * n + n_sem], refs[2 * n + n_sem:2 * n + 2 * n_sem]
        x, y, c, chips = _place()
        for k in range(n):
            for j, (px, py) in enumerate(chips):
                cp = pltpu.make_async_remote_copy(
                    src_ref=ins[k].at[2 * px + py], dst_ref=lnd[k].at[2 * px + py],
                    send_sem=s_sems[3 * k + j], recv_sem=r_sems[3 * k + j],
                    device_id=(px, py, c), device_id_type=MESH)
                cp.wait_send()
                cp.wait_recv()

    out = pl.pallas_call(
        body, name=name,
        out_shape=tuple(pltpu.HBM(a.shape, a.dtype) for a in list(arrs) + list(lands)),
        in_specs=[HBM_SPEC] * (2 * n) + [SEM_SPEC] * (2 * n_sem) + [ANY_SPEC] * len(after),
        out_specs=[HBM_SPEC] * (2 * n),
        input_output_aliases={i: i for i in range(2 * n)},
        compiler_params=pltpu.CompilerParams(has_side_effects=EFFECT),
    )(*arrs, *lands, *sems, *after)
    return list(out[:n]), list(out[n:])


def _sum_owner(chip_arr, pairs, got, name):
    nb, h, cols = got.shape
    tr = _row_tile(h, 16, 512)

    def body(chip_ref, own_ref, a_ref, b_ref, c_ref, o_ref):
        o_ref[...] = ((own_ref[0].astype(F32) + a_ref[0].astype(F32)) + b_ref[0].astype(F32)) + c_ref[0].astype(F32)

    def slot(off):
        return pl.BlockSpec((1, tr, cols), lambda i, chip_ref: ((chip_ref[0] + off) % N_CHIP, i, 0))

    return pl.pallas_call(
        body, name=name,
        grid_spec=pltpu.PrefetchScalarGridSpec(
            num_scalar_prefetch=1, grid=(h // tr,),
            in_specs=[slot(0), slot(1), slot(2), slot(3)],
            out_specs=pl.BlockSpec((tr, cols), lambda i, chip_ref: (i, 0))),
        out_shape=jax.ShapeDtypeStruct((h, cols), F32),
        compiler_params=_cparams(dimension_semantics=("parallel",)),
    )(chip_arr, pairs, got, got, got)


def _swap_start(arrs, name, after=()):
    n = len(arrs)
    first = 2 * n + len(after)

    def body(*refs):
        ins, lnd = refs[:n], refs[n:2 * n]
        send_sems, recv_sems = refs[first:first + n], refs[first + n:first + 2 * n]
        x, y, c, _ = _place()
        for k in range(n):
            pltpu.make_async_remote_copy(
                src_ref=ins[k], dst_ref=lnd[k], send_sem=send_sems[k], recv_sem=recv_sems[k],
                device_id=(x, y, 1 - c), device_id_type=MESH).start()

    lands = [_hbm(lax.empty(a.shape, a.dtype)) for a in arrs]
    out = pl.pallas_call(
        body, name=name,
        out_shape=(*[pltpu.SemaphoreType.DMA(())] * (2 * n), *[pltpu.HBM(a.shape, a.dtype) for a in arrs],
                   *[pltpu.HBM(a.shape, a.dtype) for a in arrs]),
        in_specs=[HBM_SPEC] * (2 * n) + [ANY_SPEC] * len(after),
        out_specs=(*[SEM_SPEC] * (2 * n), *[HBM_SPEC] * (2 * n)),
        input_output_aliases={i: 2 * n + i for i in range(2 * n)},
        compiler_params=pltpu.CompilerParams(has_side_effects=EFFECT),
    )(*[_hbm(a) for a in arrs], *lands, *after)
    return list(out[:2 * n]), list(out[2 * n:3 * n]), list(out[3 * n:4 * n])


def _swap_wait(sems, arrs, lands, after, name):
    n = len(arrs)

    def body(*refs):
        ins, lnd = refs[:n], refs[n:2 * n]
        s_sems, r_sems = refs[2 * n:3 * n], refs[3 * n:4 * n]
        x, y, c, _ = _place()
        for k in range(n):
            cp = pltpu.make_async_remote_copy(
                src_ref=ins[k], dst_ref=lnd[k], send_sem=s_sems[k], recv_sem=r_sems[k],
                device_id=(x, y, 1 - c), device_id_type=MESH)
            cp.wait_send()
            cp.wait_recv()

    out = pl.pallas_call(
        body, name=name,
        out_shape=tuple(pltpu.HBM(a.shape, a.dtype) for a in list(arrs) + list(lands)),
        in_specs=[HBM_SPEC] * (2 * n) + [SEM_SPEC] * (2 * n) + [ANY_SPEC] * len(after),
        out_specs=[HBM_SPEC] * (2 * n),
        input_output_aliases={i: i for i in range(2 * n)},
        compiler_params=pltpu.CompilerParams(has_side_effects=EFFECT),
    )(*arrs, *lands, *sems, *after)
    return list(out[:n]), list(out[n:])


def _row_tile(h, mult=8, cap=512):
    for t in range(cap - cap % mult, mult - 1, -mult):
        if h % t == 0:
            return t
    if mult > 8:
        return _row_tile(h, 8, cap)
    raise ValueError(h)


def _pair_sum(c_arr, full, recv, name):
    nb, rows, cols = full.shape

    def body(c_ref, f_ref, r_ref, o_ref):
        o_ref[...] = (f_ref[...] + r_ref[...]).astype(BF16)

    if _halved_by_rows(full.shape):
        h = rows // 2
        tr = _row_tile(h, 16, 512)
        steps = h // tr
        own = pl.BlockSpec((1, tr, cols), lambda b, i, c_ref: (b, c_ref[0] * steps + i, 0))
        half = pl.BlockSpec((1, tr, cols), lambda b, i, c_ref: (b, i, 0))
    else:
        steps = 1
        own = pl.BlockSpec((1, rows, cols // 2), lambda b, i, c_ref: (b, 0, c_ref[0]))
        half = pl.BlockSpec((1, rows, cols // 2), lambda b, i, c_ref: (b, 0, 0))
    return pl.pallas_call(
        body, name=name,
        grid_spec=pltpu.PrefetchScalarGridSpec(
            num_scalar_prefetch=1, grid=(nb, steps), in_specs=[own, half], out_specs=half),
        out_shape=jax.ShapeDtypeStruct(_half_shape(full.shape), BF16),
        compiler_params=_cparams(dimension_semantics=("parallel", "parallel")),
    )(c_arr, full, recv)


def _adam_math(g, w, m, v):
    m1 = ADAM_B1 * m + (1.0 - ADAM_B1) * g
    v1 = ADAM_B2 * v + (1.0 - ADAM_B2) * (g * g)
    m_hat = m1 / (1.0 - ADAM_B1 ** ADAM_STEP)
    v_hat = v1 / (1.0 - ADAM_B2 ** ADAM_STEP)
    delta = -ADAM_LR * (m_hat / (jnp.sqrt(v_hat) + ADAM_EPS) + ADAM_WD * w)
    return delta, m1, v1


def _adamw_halves(c_arr, own, other, w, m, v, name):
    rows, cols = w.shape
    by_rows = own.shape[1] == cols

    def body(c_ref, own_ref, oth_ref, w_ref, m_ref, v_ref, g_out, d_out, m_out, v_out):
        if by_rows:
            g = jnp.where(pl.program_id(0) == c_ref[0], own_ref[...], oth_ref[...])
        else:
            own_, oth_ = own_ref[...], oth_ref[...]
            g = jnp.where(c_ref[0] == 0, jnp.concatenate([own_, oth_], axis=1), jnp.concatenate([oth_, own_], axis=1))
        d, m1, v1 = _adam_math(g, w_ref[...], m_ref[...], v_ref[...])
        g_out[...] = g
        d_out[...] = d
        m_out[...] = m1
        v_out[...] = v1

    if by_rows:
        h = rows // 2
        tr = _row_tile(h)
        steps = h // tr
        grid = (2, steps)
        half_spec = pl.BlockSpec((tr, cols), lambda p, i, c_ref: (i, 0))
        full_spec = pl.BlockSpec((tr, cols), lambda p, i, c_ref: (p * steps + i, 0))
    else:
        tr = _row_tile(rows)
        grid = (1, rows // tr)
        half_spec = pl.BlockSpec((tr, cols // 2), lambda p, i, c_ref: (i, 0))
        full_spec = pl.BlockSpec((tr, cols), lambda p, i, c_ref: (i, 0))
    return pl.pallas_call(
        body, name=name,
        grid_spec=pltpu.PrefetchScalarGridSpec(
            num_scalar_prefetch=1, grid=grid,
            in_specs=[half_spec, half_spec, full_spec, full_spec, full_spec],
            out_specs=[full_spec] * 4),
        out_shape=[jax.ShapeDtypeStruct(w.shape, F32)] * 4,
        compiler_params=_cparams(dimension_semantics=("parallel", "parallel")),
    )(c_arr, own, other, w, m, v)


def _adamw_whole(items, name):
    n = len(items)

    def body(*refs):
        ins, outs = refs[:4 * n], refs[4 * n:]
        for k in range(n):
            g, w, m, v = (r[...] for r in ins[4 * k:4 * k + 4])
            d, m1, v1 = _adam_math(g, w, m, v)
            outs[3 * k][...] = d
            outs[3 * k + 1][...] = m1
            outs[3 * k + 2][...] = v1

    flat = [a for it in items for a in it]
    shapes = [jax.ShapeDtypeStruct(it[1].shape, F32) for it in items for _ in range(3)]
    out = pl.pallas_call(
        body, name=name, out_shape=shapes,
        in_specs=[VMEM_SPEC] * (4 * n), out_specs=[VMEM_SPEC] * (3 * n),
        compiler_params=_cparams(),
    )(*flat)
    return [tuple(out[3 * k:3 * k + 3]) for k in range(n)]


def _adamw_tiled(g, w, m, v, name):
    rows, cols = w.shape
    tr = _row_tile(rows)

    def body(g_ref, w_ref, m_ref, v_ref, d_out, m_out, v_out):
        d, m1, v1 = _adam_math(g_ref[...], w_ref[...], m_ref[...], v_ref[...])
        d_out[...] = d
        m_out[...] = m1
        v_out[...] = v1

    spec = pl.BlockSpec((tr, cols), lambda i: (i, 0))
    return pl.pallas_call(
        body, name=name, grid=(rows // tr,),
        out_shape=[jax.ShapeDtypeStruct(w.shape, F32)] * 3,
        in_specs=[spec] * 4, out_specs=[spec] * 3,
        compiler_params=_cparams(dimension_semantics=("parallel",)),
    )(g, w, m, v)


def _mod_forward(cond, w_mod, b_mod_cols, name):
    def body(c_ref, w_ref, b_ref, o_ref):
        o_ref[...] = _dot(_silu(c_ref[...]), w_ref[...]) + b_ref[...]

    return pl.pallas_call(
        body, name=name, out_shape=jax.ShapeDtypeStruct((cond.shape[0], w_mod.shape[1]), F32),
        in_specs=[VMEM_SPEC] * 3, out_specs=VMEM_SPEC, compiler_params=_cparams(),
    )(cond, w_mod, b_mod_cols)


def _mod_backward(cond, w_mod, dmod_cols, name):
    def body(c_ref, w_ref, d_ref, gw_ref, gc_ref):
        s = _silu(c_ref[...])
        d = d_ref[...]
        gw_ref[...] = _dot_tn(s, d)
        gc_ref[...] = _dot_nt(d[8:16, :], w_ref[...])

    return pl.pallas_call(
        body, name=name,
        out_shape=[jax.ShapeDtypeStruct(w_mod.shape, F32), jax.ShapeDtypeStruct((8, w_mod.shape[0]), F32)],
        in_specs=[VMEM_SPEC] * 3, out_specs=[VMEM_SPEC] * 2, compiler_params=_cparams(),
    )(cond, w_mod, dmod_cols)


def _col_chunks(width, step=512):
    return [(s, min(step, width - s)) for s in range(0, width, step)]


def _w_in_row(p_off):
    if p_off < 9 * HW:
        return p_off
    return 9 * HW if p_off == OFF_LR else p_off + 2 * RANK


def _in_projection(ctx0, x0, modc, modx, pre1, w_t, n_ctx_tiles, name):
    d = x0.shape[1]
    rows = ctx0.shape[0] + x0.shape[0]
    width = P_WIDTH

    def body(ctx_ref, x_ref, modc_ref, modx_ref, pre_ref, w_ref, h_ref, p_ref):
        is_ctx = pl.program_id(0) < n_ctx_tiles
        n, _ = _rms(jnp.where(is_ctx, ctx_ref[...], x_ref[...]))
        shift = jnp.where(is_ctx, modc_ref[0:1, :], modx_ref[0:1, :])
        scale = jnp.where(is_ctx, modc_ref[1:2, :], modx_ref[1:2, :])
        h = (n * pre_ref[...] * (1.0 + scale) + shift).astype(BF16)
        h_ref[...] = h
        for s, w in _col_chunks(width):
            p_ref[:, s:s + w] = _dot_nt(h, w_ref[_w_in_row(s):_w_in_row(s) + w, :])

    row = lambda i: (i, 0)
    fixed = lambda i: (0, 0)
    return pl.pallas_call(
        body, name=name, grid=(rows // TM,),
        out_shape=[jax.ShapeDtypeStruct((rows, d), BF16), jax.ShapeDtypeStruct((rows, width), F32)],
        in_specs=[pl.BlockSpec((TM, d), lambda i: (jnp.minimum(i, n_ctx_tiles - 1), 0)),
                  pl.BlockSpec((TM, d), lambda i: (jnp.maximum(i - n_ctx_tiles, 0), 0)),
                  pl.BlockSpec((8, d), fixed), pl.BlockSpec((8, d), fixed), pl.BlockSpec((1, d), fixed), VMEM_SPEC],
        out_specs=[pl.BlockSpec((TM, d), row), pl.BlockSpec((TM, width), row)],
        compiler_params=_cparams(dimension_semantics=("parallel",)),
    )(ctx0, x0, modc, modx, pre1, w_t)


C_HQ, C_HI, C_HF_FW, C_HF_BW, C_HGATE, C_GQ, C_GK, C_GV, C_GGATE = range(9)
OFF_GATE_HG = 9 * HW
OFF_LR = 13 * HW
P_WIDTH = OFF_LR + 128


def _head_norm_fwd(o, w):
    outs, ns, rs = [], [], []
    for h in range(NH):
        n, r = _rms(o[:, h * HD:(h + 1) * HD])
        ns.append(n)
        rs.append(r)
        outs.append(n * w)
    return jnp.concatenate(outs, axis=1), ns, rs


def _mixer_tail(z, o_hg, o_gla, p_hgate, p_ggate, p_gate_hg, p_gate_gla, hg_on, gla_on, wbh, wbg, wout):
    on_hg, n_hg, r_hg = _head_norm_fwd(o_hg, hg_on)
    on_gla, n_gla, r_gla = _head_norm_fwd(o_gla, gla_on)
    og_hg = (on_hg * _silu(p_hgate)).astype(BF16)
    og_gla = (on_gla * _silu(p_ggate)).astype(BF16)
    b_hg = jnp.dot(og_hg, wbh, preferred_element_type=F32)
    b_gla = jnp.dot(og_gla, wbg, preferred_element_type=F32)
    s_hg = _sigmoid(p_gate_hg)
    s_gla = _sigmoid(p_gate_gla)
    merged = (s_hg * b_hg + s_gla * b_gla).astype(BF16)
    y1 = jnp.dot(merged, wout, preferred_element_type=F32)
    return dict(on_hg=on_hg, n_hg=n_hg, r_hg=r_hg, on_gla=on_gla, n_gla=n_gla, r_gla=r_gla, og_hg=og_hg,
                og_gla=og_gla, b_hg=b_hg, b_gla=b_gla, s_hg=s_hg, s_gla=s_gla, merged=merged, y1=y1)


def _mixer_tail_fwd(x_lat, p, o_list, modx, norms, onorms, w_br_hg, w_br_gla, w_out, n_ctx_tiles, name):
    rows, d = x_lat.shape

    def body(x_ref, ofw_hg, obw_hg, ofw_gla, obw_gla, p_hgate, p_ggate, p_ghg_a, p_ghg_b, p_ggla_a, p_ggla_b,
             modx_ref, norm_ref, on_ref, wbh_ref, wbg_ref, wout_ref, z2_ref, y1_ref, mrg_ref, oghg_ref, oggla_ref):
        p_gate_hg = jnp.concatenate([p_ghg_a[...], p_ghg_b[...]], axis=1)
        p_gate_gla = jnp.concatenate([p_ggla_a[...], p_ggla_b[...]], axis=1)
        t = _mixer_tail(x_ref[...], ofw_hg[...] + obw_hg[...], ofw_gla[...] + obw_gla[...], p_hgate[...],
                        p_ggate[...], p_gate_hg, p_gate_gla, on_ref[0:1, 0:HD], on_ref[1:2, 0:HD],
                        wbh_ref[...], wbg_ref[...], wout_ref[...])
        y1_ref[...] = t["y1"]
        mrg_ref[...] = t["merged"]
        oghg_ref[...] = t["og_hg"]
        oggla_ref[...] = t["og_gla"]
        n1, _ = _rms(t["y1"])
        z2_ref[...] = x_ref[...] + n1 * norm_ref[1:2, :] * modx_ref[2:3, :]

    lat = lambda i: (i, 0)
    full = lambda i: (i + n_ctx_tiles, 0)
    fixed = lambda i: (0, 0)

    def pcol(blk):
        return pl.BlockSpec((TM, HW), lambda i: (i + n_ctx_tiles, blk))

    in_specs = ([pl.BlockSpec((TM, d), lat)] + [pl.BlockSpec((TM, HW), full)] * 4
                + [pcol(C_HGATE), pcol(C_GGATE), pcol(9), pcol(10), pcol(11), pcol(12)]
                + [pl.BlockSpec((8, d), fixed)] * 3 + [VMEM_SPEC] * 3)
    bf = lambda w: jax.ShapeDtypeStruct((rows, w), BF16)
    f32 = jax.ShapeDtypeStruct((rows, d), F32)
    return pl.pallas_call(
        body, name=name, grid=(rows // TM,), out_shape=[f32, f32, bf(d), bf(HW), bf(HW)], in_specs=in_specs,
        out_specs=[pl.BlockSpec((TM, d), lat)] * 3 + [pl.BlockSpec((TM, HW), lat)] * 2,
        compiler_params=_cparams(dimension_semantics=("parallel",)),
    )(x_lat, *o_list, p, p, p, p, p, p, modx, norms, onorms, w_br_hg, w_br_gla, w_out)


def _ffn_fwd_bwd(z2, modx, norms, w_gate, w_up, w_down, target, name):
    rows, d = z2.shape
    dff = w_gate.shape[0]
    inv_d = 1.0 / d

    def body(z2_ref, modx_ref, norm_ref, wg_ref, wu_ref, wd_ref, t_ref,
             loss_ref, dz2_ref, h2_ref, a_ref, du_ref, dv_ref, dy2_ref, stat_ref):
        i = pl.program_id(0)
        pre2, post2 = norm_ref[2:3, :], norm_ref[3:4, :]
        shift2, scale2, gate2 = modx_ref[3:4, :], modx_ref[4:5, :], modx_ref[5:6, :]
        z2 = z2_ref[...]
        n2, r2 = _rms(z2)
        nw2 = n2 * pre2
        h2 = (nw2 * (1.0 + scale2) + shift2).astype(BF16)
        h2_ref[...] = h2
        u = _dot_nt(h2, wg_ref[...])
        v = _dot_nt(h2, wu_ref[...])
        su = _silu(u)
        a = (su * v).astype(BF16)
        a_ref[...] = a
        y2 = jnp.dot(a, wd_ref[...], preferred_element_type=F32)
        n3, r3 = _rms(y2)
        err = z2 + n3 * post2 * gate2 - t_ref[...]
        part = 0.5 * inv_d * jnp.sum(err * err)
        dz3 = err * inv_d
        dgate2 = _colsum(dz3 * n3 * post2)
        tt = dz3 * gate2
        dpost2 = _colsum(tt * n3)
        dy2 = _rms_bwd(tt * post2, n3, r3).astype(BF16)
        dy2_ref[...] = dy2
        da = _dot_nt(dy2, wd_ref[...])
        du = (da * v * _dsilu(u)).astype(BF16)
        dv = (da * su).astype(BF16)
        du_ref[...] = du
        dv_ref[...] = dv
        dh2 = (jnp.dot(du, wg_ref[...], preferred_element_type=F32)
               + jnp.dot(dv, wu_ref[...], preferred_element_type=F32))
        dshift2 = _colsum(dh2)
        dscale2 = _colsum(dh2 * nw2)
        dnw2 = dh2 * (1.0 + scale2)
        dpre2 = _colsum(dnw2 * n2)
        dz2_ref[...] = dz3 + _rms_bwd(dnw2 * pre2, n2, r2)

        @pl.when(i == 0)
        def _():
            stat_ref[...] = jnp.zeros_like(stat_ref)
            loss_ref[...] = jnp.zeros_like(loss_ref)

        for r, val in enumerate((dshift2, dscale2, dgate2, dpre2, dpost2)):
            stat_ref[r:r + 1, :] += val
        loss_ref[...] += part
        stat_ref[5:6, 0:128] += part

    lat = lambda i: (i, 0)
    fixed = lambda i: (0, 0)
    bf = lambda w: jax.ShapeDtypeStruct((rows, w), BF16)
    return pl.pallas_call(
        body, name=name, grid=(rows // TM,),
        out_shape=[jax.ShapeDtypeStruct((8, 128), F32), jax.ShapeDtypeStruct((rows, d), F32), bf(d), bf(dff), bf(dff),
                   bf(dff), bf(d), jax.ShapeDtypeStruct((8, d), F32)],
        in_specs=[pl.BlockSpec((TM, d), lat), pl.BlockSpec((8, d), fixed), pl.BlockSpec((8, d), fixed)]
        + [VMEM_SPEC] * 3 + [pl.BlockSpec((TM, d), lat)],
        out_specs=[pl.BlockSpec((8, 128), fixed), pl.BlockSpec((TM, d), lat), pl.BlockSpec((TM, d), lat),
                   pl.BlockSpec((TM, dff), lat), pl.BlockSpec((TM, dff), lat), pl.BlockSpec((TM, dff), lat),
                   pl.BlockSpec((TM, d), lat), pl.BlockSpec((8, d), fixed)],
        compiler_params=_cparams(dimension_semantics=("arbitrary",)),
    )(z2, modx, norms, w_gate, w_up, w_down, target)


def _mixer_tail_bwd(x_lat, p, o_list, dz2, y1, modx, norms, onorms, w_br_hg, w_br_gla, w_out, n_ctx_tiles, n_tiles,
                    name):
    rows, d = x_lat.shape
    total = n_tiles * TM

    def body(x_ref, ofw_hg, obw_hg, ofw_gla, obw_gla, p_hgate, p_ggate, p_ghg_a, p_ghg_b, p_ggla_a, p_ggla_b,
             dz2_ref, y1_ref, modx_ref, norm_ref, on_ref, wbh_ref, wbg_ref, wout_ref,
             dohg_ref, dogla_ref, dhgate_ref, dggate_ref, dghg_ref, dggla_ref, dy1_ref, dbhg_ref, dbgla_ref,
             stat_ref):
        i = pl.program_id(0)

        @pl.when(i == 0)
        def _():
            stat_ref[...] = jnp.zeros_like(stat_ref)

        @pl.when(i < n_ctx_tiles)
        def _():
            for ref in (dohg_ref, dogla_ref, dhgate_ref, dggate_ref, dghg_ref, dggla_ref):
                ref[...] = jnp.zeros_like(ref)

        @pl.when(i >= n_ctx_tiles)
        def _():
            post1, gate1 = norm_ref[1:2, :], modx_ref[2:3, :]
            hg_on, gla_on = on_ref[0:1, 0:HD], on_ref[1:2, 0:HD]
            p_gate_hg = jnp.concatenate([p_ghg_a[...], p_ghg_b[...]], axis=1)
            p_gate_gla = jnp.concatenate([p_ggla_a[...], p_ggla_b[...]], axis=1)
            ph, pg = p_hgate[...], p_ggate[...]
            t = _mixer_tail(x_ref[...], ofw_hg[...] + obw_hg[...], ofw_gla[...] + obw_gla[...], ph, pg,
                            p_gate_hg, p_gate_gla, hg_on, gla_on, wbh_ref[...], wbg_ref[...], wout_ref[...])
            dz2 = dz2_ref[...]
            n1, r1 = _rms(y1_ref[...])
            dgate1 = _colsum(dz2 * n1 * post1)
            tt = dz2 * gate1
            dpost1 = _colsum(tt * n1)
            dy1 = _rms_bwd(tt * post1, n1, r1).astype(BF16)
            dy1_ref[...] = dy1
            dmerged = _dot_nt(dy1, wout_ref[...])
            dghg_ref[...] = (dmerged * t["b_hg"] * t["s_hg"] * (1.0 - t["s_hg"])).astype(BF16)
            dggla_ref[...] = (dmerged * t["b_gla"] * t["s_gla"] * (1.0 - t["s_gla"])).astype(BF16)
            db_hg = (dmerged * t["s_hg"]).astype(BF16)
            db_gla = (dmerged * t["s_gla"]).astype(BF16)
            dbhg_ref[...] = db_hg
            dbgla_ref[...] = db_gla
            don_acc = []
            for (db, wb, pgate, on, ns, rs, gain, gate_ref, do_ref) in (
                    (db_hg, wbh_ref, ph, t["on_hg"], t["n_hg"], t["r_hg"], hg_on, dhgate_ref, dohg_ref),
                    (db_gla, wbg_ref, pg, t["on_gla"], t["n_gla"], t["r_gla"], gla_on, dggate_ref, dogla_ref)):
                dog = _dot_nt(db, wb[...])
                gate_ref[...] = (dog * on * _dsilu(pgate)).astype(BF16)
                don = dog * _silu(pgate)
                acc = jnp.zeros((1, HD), F32)
                for h in range(NH):
                    sl = slice(h * HD, (h + 1) * HD)
                    acc = acc + _colsum(don[:, sl] * ns[h])
                    do_ref[:, sl] = _rms_bwd(don[:, sl] * gain, ns[h], rs[h]).astype(BF16)
                don_acc.append(acc)
            stat_ref[0:1, :] += dgate1
            stat_ref[1:2, :] += dpost1
            stat_ref[2:3, 0:HD] += don_acc[0]
            stat_ref[2:3, HD:2 * HD] += don_acc[1]

    lat = lambda i: (jnp.maximum(i - n_ctx_tiles, 0), 0)
    full = lambda i: (i, 0)
    fixed = lambda i: (0, 0)

    def pcol(blk):
        return pl.BlockSpec((TM, HW), lambda i: (i, blk))

    in_specs = ([pl.BlockSpec((TM, d), lat)] + [pl.BlockSpec((TM, HW), full)] * 4
                + [pcol(C_HGATE), pcol(C_GGATE), pcol(9), pcol(10), pcol(11), pcol(12)]
                + [pl.BlockSpec((TM, d), lat), pl.BlockSpec((TM, d), lat)]
                + [pl.BlockSpec((8, d), fixed)] * 3 + [VMEM_SPEC] * 3)
    f = lambda w: jax.ShapeDtypeStruct((total, w), BF16)
    out_shape = [f(HW), f(HW), f(HW), f(HW), f(d), f(d), jax.ShapeDtypeStruct((rows, d), BF16),
                 jax.ShapeDtypeStruct((rows, d), BF16), jax.ShapeDtypeStruct((rows, d), BF16),
                 jax.ShapeDtypeStruct((8, d), F32)]
    out_specs = ([pl.BlockSpec((TM, HW), full)] * 4 + [pl.BlockSpec((TM, d), full)] * 2
                 + [pl.BlockSpec((TM, d), lat)] * 3 + [pl.BlockSpec((8, d), fixed)])
    return pl.pallas_call(
        body, name=name, grid=(n_tiles,), out_shape=out_shape, in_specs=in_specs, out_specs=out_specs,
        compiler_params=_cparams(dimension_semantics=("arbitrary",)),
    )(x_lat, *o_list, p, p, p, p, p, p, dz2, y1, modx, norms, onorms, w_br_hg, w_br_gla, w_out)


def _in_projection_bwd(ctx0, x0, dz2, modc, modx, pre1, w_t, pieces, n_ctx_tiles, name):
    d = x0.shape[1]
    rows = ctx0.shape[0] + x0.shape[0]
    lat_rows = dz2.shape[0]
    width = P_WIDTH
    n_pieces = len(pieces)

    def body(*refs):
        ctx_ref, x_ref, dz2_ref, modc_ref, modx_ref, pre_ref, w_ref = refs[:7]
        (dhq_f, dhq_b, dhi_f, dhi_b, dhf_f, dhf_b, dhgate, dgq_f, dgq_b, dgk_f, dgk_b, dgv_f, dgv_b, dggate,
         dghg, dggla, dlr_f, dlr_b) = refs[7:7 + n_pieces]
        dp_ref, gx_ref, stat_ref = refs[7 + n_pieces:]
        i = pl.program_id(0)
        is_ctx = i < n_ctx_tiles
        z = jnp.where(is_ctx, ctx_ref[...], x_ref[...])
        sections = [
            (0, dhq_f[...] + dhq_b[...]), (HW, dhi_f[...] + dhi_b[...]), (2 * HW, dhf_f[...]), (3 * HW, dhf_b[...]),
            (4 * HW, dhgate[...]), (5 * HW, dgq_f[...] + dgq_b[...]), (6 * HW, dgk_f[...] + dgk_b[...]),
            (7 * HW, dgv_f[...] + dgv_b[...]), (8 * HW, dggate[...]),
            (9 * HW, dghg[:, 0:HW]), (10 * HW, dghg[:, HW:2 * HW]),
            (11 * HW, dggla[:, 0:HW]), (12 * HW, dggla[:, HW:2 * HW]), (OFF_LR, dlr_f[...] + dlr_b[...])]
        dh = jnp.zeros((TM, d), F32)
        for off, val in sections:
            w = val.shape[1]
            vb = val.astype(BF16)
            dp_ref[off:off + w, :] = vb.T
            dh = dh + jnp.dot(vb, w_ref[_w_in_row(off):_w_in_row(off) + w, :], preferred_element_type=F32)
        n, r = _rms(z)
        pre = pre_ref[...]
        scale = jnp.where(is_ctx, modc_ref[1:2, :], modx_ref[1:2, :])
        nw = n * pre
        dshift = _colsum(dh)
        dscale = _colsum(dh * nw)
        dnw = dh * (1.0 + scale)
        dpre = _colsum(dnw * n)
        gx_ref[...] = dz2_ref[...] + _rms_bwd(dnw * pre, n, r)
        zero = jnp.zeros((1, d), F32)

        @pl.when(i == 0)
        def _():
            stat_ref[...] = jnp.zeros_like(stat_ref)

        stat_ref[0:1, :] += jnp.where(is_ctx, zero, dshift)
        stat_ref[1:2, :] += jnp.where(is_ctx, zero, dscale)
        stat_ref[2:3, :] += jnp.where(is_ctx, dshift, zero)
        stat_ref[3:4, :] += jnp.where(is_ctx, dscale, zero)
        stat_ref[4:5, :] += dpre

    full = lambda i: (i, 0)
    lat = lambda i: (jnp.maximum(i - n_ctx_tiles, 0), 0)
    fixed = lambda i: (0, 0)
    piece_specs = [pl.BlockSpec((TM, a.shape[1]), full) for a in pieces]
    in_specs = [pl.BlockSpec((TM, d), lambda i: (jnp.minimum(i, n_ctx_tiles - 1), 0)), pl.BlockSpec((TM, d), lat),
                pl.BlockSpec((TM, d), lat), pl.BlockSpec((8, d), fixed),
                pl.BlockSpec((8, d), fixed), pl.BlockSpec((1, d), fixed), VMEM_SPEC] + piece_specs
    return pl.pallas_call(
        body, name=name, grid=(rows // TM,),
        out_shape=[jax.ShapeDtypeStruct((width, rows), BF16), jax.ShapeDtypeStruct((lat_rows, d), F32),
                   jax.ShapeDtypeStruct((8, d), F32)],
        in_specs=in_specs,
        out_specs=[pl.BlockSpec((width, TM), lambda i: (0, i)), pl.BlockSpec((TM, d), lat),
                   pl.BlockSpec((8, d), fixed)],
        compiler_params=_cparams(dimension_semantics=("arbitrary",)),
    )(ctx0, x0, dz2, modc, modx, pre1, w_t, *pieces)


def _transposed_lhs_matmul(x_ref, dy_ref, o_ref, xt_ref):
    @pl.when(pl.program_id(1) == 0)
    def _():
        xt_ref[...] = x_ref[...].T

    o_ref[...] = jnp.dot(xt_ref[...], dy_ref[...], preferred_element_type=F32)


def _w_in_grad(dp_t, h1, n_cols, name, after=()):
    rows, d = h1.shape
    n_main = OFF_LR // HW
    lr0 = _w_in_row(OFF_LR)

    def body(x_ref, xlr_ref, h_ref, *rest):
        o_hbm, acc_ref, sems = rest[len(after):]
        i = pl.program_id(0)
        slot = i % 2

        def main_copy(step):
            row = jnp.where(step < 9, step * HW, step * HW + 2 * RANK)
            return pltpu.make_async_copy(acc_ref.at[step % 2], o_hbm.at[pl.ds(pl.multiple_of(row, 8), HW), :],
                                         sems.at[step % 2])

        @pl.when(i > 1)
        def _():
            main_copy(i - 2).wait()

        @pl.when(i < n_main)
        def _():
            acc_ref[slot] = jnp.dot(x_ref[...], h_ref[...], preferred_element_type=F32)
            main_copy(i).start()

        @pl.when(i == n_main)
        def _():
            acc_ref[slot, 0:128, :] = jnp.dot(xlr_ref[...], h_ref[...], preferred_element_type=F32)
            lr_copy = pltpu.make_async_copy(acc_ref.at[slot, 0:2 * RANK, :], o_hbm.at[lr0:lr0 + 2 * RANK, :],
                                            sems.at[slot])
            lr_copy.start()
            main_copy(i - 1).wait()
            lr_copy.wait()

    return pl.pallas_call(
        body, name=name, grid=(n_main + 1,),
        out_shape=jax.ShapeDtypeStruct((n_cols, d), F32),
        in_specs=[pl.BlockSpec((HW, rows), lambda i: (jnp.minimum(i, n_main - 1), 0)),
                  pl.BlockSpec((128, rows), lambda i: (OFF_LR // 128, 0)),
                  pl.BlockSpec((rows, d), lambda i: (0, 0))] + [ANY_SPEC] * len(after),
        out_specs=ANY_SPEC,
        scratch_shapes=[pltpu.VMEM((2, HW, d), F32), pltpu.SemaphoreType.DMA((2,))],
        compiler_params=_cparams(dimension_semantics=("arbitrary",)),
    )(dp_t, dp_t, h1, *after)


def _weight_grad(xs, dy, name, tk=None, tn=512, k_first=0, k_tiles=None):
    rows = dy.shape[0]
    n = dy.shape[1]
    tn_ = min(tn, n)
    tk_ = xs.shape[1] if tk is None else tk
    k_tiles = xs.shape[1] // tk_ if k_tiles is None else k_tiles
    k = k_tiles * tk_

    return pl.pallas_call(
        functools.partial(_transposed_lhs_matmul), name=name, grid=(k_tiles, n // tn_),
        out_shape=jax.ShapeDtypeStruct((k, n), F32),
        in_specs=[pl.BlockSpec((rows, tk_), lambda i, j: (0, i + k_first)),
                  pl.BlockSpec((rows, tn_), lambda i, j: (0, j))],
        out_specs=pl.BlockSpec((tk_, tn_), lambda i, j: (i, j)),
        scratch_shapes=[pltpu.VMEM((tk_, rows), BF16)],
        compiler_params=_cparams(dimension_semantics=("parallel", "arbitrary")),
    )(xs, dy)


def _running_sums(xs, fws):
    c = xs[0].shape[0]
    row = lax.broadcasted_iota(jnp.int32, (c, 1), 0)
    s = 1
    while s < c:
        xs = [x + (jnp.where(row >= s, pltpu.roll(x, s, axis=0), 0.0) if fw else
                   jnp.where(row < c - s, pltpu.roll(x, c - s, axis=0), 0.0)) for x, fw in zip(xs, fws)]
        s *= 2
    return xs


def _chunks_terms(qs, ks, gs, fws):
    c = CHUNK
    n = len(qs)
    r = lax.broadcasted_iota(jnp.int32, (c, c), 0)
    s = lax.broadcasted_iota(jnp.int32, (c, c), 1)
    row = lax.broadcasted_iota(jnp.int32, (c, 1), 0)
    per_dir = {}
    for fw in set(fws):
        pos = row if fw else (c - 1 - row)
        per_dir[fw] = dict(
            causal=(s <= r) if fw else (s >= r), causal_t=(s >= r) if fw else (s <= r), pos=pos,
            in_blk=[(pos >= SUB * j) & (pos < SUB * (j + 1)) for j in range(NSUB)],
            start_row=[None] + [SUB * j - 1 if fw else c - SUB * j for j in range(1, NSUB)],
            rend=c - 1 if fw else 0)
    dirs = [per_dir[fw] for fw in fws]
    cums = _running_sums(gs, fws)
    starts = [[None] + [cum[d["start_row"][j]:d["start_row"][j] + 1, :] for j in range(1, NSUB)]
              for cum, d in zip(cums, dirs)]
    es = [[jnp.exp(cum) for cum in cums]]
    for j in range(1, NSUB):
        es.append([jnp.exp(jnp.where(d["pos"] >= SUB * j, cum - st[j], -1e30)) for cum, st, d in zip(cums, starts, dirs)])
    owns = [functools.reduce(lambda rest, j: jnp.where(d["in_blk"][j], st[j], rest), range(1, NSUB), 0.0)
            for st, d in zip(starts, dirs)]
    kscales = [jnp.exp(own - cum) for own, cum in zip(owns, cums)]
    cends = [cum[d["rend"]:d["rend"] + 1, :] for cum, d in zip(cums, dirs)]
    tails = [jnp.exp(cend - cum) for cend, cum in zip(cends, cums)]
    qcats = [jnp.concatenate([q * es[j][i] for j in range(NSUB)], axis=1).astype(BF16) for i, q in enumerate(qs)]
    kts = [k * ksc for k, ksc in zip(ks, kscales)]
    kms = [jnp.concatenate([jnp.where(d["in_blk"][j], kt, 0.0) for j in range(NSUB)], axis=1).astype(BF16)
           for kt, d in zip(kts, dirs)]
    e_by_lane = [[es[j][i] for j in range(NSUB)] for i in range(n)]
    return dict(dirs=dirs, e=e_by_lane, kscale=kscales, cend=cends, tail=tails, qcat=qcats, km=kms, kt=kts)


def _chunks_fwd(qs, ks, vs, gs, st0s, fws):
    t = _chunks_terms(qs, ks, gs, fws)
    scores = [_dot_nt(qc, km) for qc, km in zip(t["qcat"], t["km"])]
    a = [jnp.where(d["causal"], sc, 0.0) for sc, d in zip(scores, t["dirs"])]
    inter = [_dot_nt(qc[:, 0:HD], st0) for qc, st0 in zip(t["qcat"], st0s)]
    intra = [_dot(a_, v) for a_, v in zip(a, vs)]
    os_ = [x + y for x, y in zip(intra, inter)]
    upd = [_dot_tn(v, k * tl) for v, k, tl in zip(vs, ks, t["tail"])]
    st1s = [st0 * jnp.exp(ce) + u for st0, ce, u in zip(st0s, t["cend"], upd)]
    return os_, st1s


def _chunks_bwd(qs, ks, vs, gs, st0s, dos, dst1s, fws):
    n = len(qs)
    t = _chunks_terms(qs, ks, gs, fws)
    qcat, km, e, dirs = t["qcat"], t["km"], t["e"], t["dirs"]
    a_t = [jnp.where(d["causal_t"], _dot_nt(km_, qc), 0.0) for km_, qc, d in zip(km, qcat, dirs)]
    ktail = [k * tl for k, tl in zip(ks, t["tail"])]
    dv_a = [_dot(at, do) for at, do in zip(a_t, dos)]
    dv_b = [_dot_nt(kt, ds) for kt, ds in zip(ktail, dst1s)]
    dv = [x + y for x, y in zip(dv_a, dv_b)]
    da = [jnp.where(d["causal"], _dot_nt(do, v), 0.0) for do, v, d in zip(dos, vs, dirs)]
    da_t = [jnp.where(d["causal_t"], _dot_nt(v, do), 0.0) for do, v, d in zip(dos, vs, dirs)]
    dqcat = [_dot(da_, km_) for da_, km_ in zip(da, km)]
    dq_inter = [e[i][0] * _dot(dos[i], st0s[i]) for i in range(n)]
    dkm = [_dot(dat, qc) for dat, qc in zip(da_t, qcat)]
    dk_inter = [_dot(v, ds) * tl for v, ds, tl in zip(vs, dst1s, t["tail"])]
    dq = [dq_inter[i] + sum(e[i][j] * dqcat[i][:, j * HD:(j + 1) * HD] for j in range(NSUB)) for i in range(n)]
    dkt = [sum(jnp.where(dirs[i]["in_blk"][j], dkm[i][:, j * HD:(j + 1) * HD], 0.0) for j in range(NSUB))
           for i in range(n)]
    dk = [dkt[i] * t["kscale"][i] + dk_inter[i] for i in range(n)]
    dcum = [qs[i] * dq_inter[i] - ks[i] * dk_inter[i] - t["kt"][i].astype(BF16).astype(F32) * dkt[i]
            + sum(qcat[i][:, j * HD:(j + 1) * HD].astype(F32) * dqcat[i][:, j * HD:(j + 1) * HD] for j in range(NSUB))
            for i in range(n)]
    ecend = [jnp.exp(ce) for ce in t["cend"]]
    end = [ecend[i] * _colsum(st0s[i] * dst1s[i]) + _colsum(ks[i] * dk_inter[i]) for i in range(n)]
    sums = _running_sums(dcum, [not fw for fw in fws])
    dg = [sm + en for sm, en in zip(sums, end)]
    upd = [_dot_tn(dos[i], qs[i] * e[i][0]) for i in range(n)]
    dst0 = [dst1s[i] * ecend[i] + upd[i] for i in range(n)]
    return dq, dk, dv, dg, dst0


def _chunk_index(step, n_ctx_chunks, n_chunks, fw):
    if fw:
        return step
    return jnp.where(step < n_ctx_chunks, n_ctx_chunks - 1 - step, n_chunks - 1 + n_ctx_chunks - step)


def _hg_inputs(hq, hf, lbv, d_idx, sl):
    lb = _sigmoid(lbv[d_idx:d_idx + 1, sl] - lbv[2 + d_idx:3 + d_idx, sl])
    sg = _sigmoid(hf)
    f = lb + (1.0 - lb) * sg
    return _silu(hq), 1.0 - f, jnp.log(f), f, sg, lb


def _scan_fwd_both(p, branch_sides, n_ctx_chunks, name):
    rows = p.shape[0]
    n_chunks = rows // CHUNK
    n_ins = [4 if branch == "hg" else 6 for branch, _ in branch_sides]
    n_in_all = 2 * sum(n_ins)
    n_br = len(branch_sides)

    def body(*refs):
        ins, outs, state = refs[:n_in_all], refs[n_in_all:n_in_all + 4 * n_br], refs[-1]

        @pl.when(pl.program_id(0) == 0)
        def _():
            state[...] = jnp.zeros_like(state)

        lanes, where = [], []
        pos = 0
        for bi, (branch, _) in enumerate(branch_sides):
            hg = branch == "hg"
            n_in = n_ins[bi]
            for di, fw in enumerate((True, False)):
                r = ins[pos:pos + n_in]
                pos += n_in
                o_ref, st_ref = outs[4 * bi + 2 * di], outs[4 * bi + 2 * di + 1]
                if hg:
                    a_ref, b_ref, c_ref, lb_ref = r
                else:
                    a_ref, b_ref, c_ref, lr_ref, wgk_ref, bgk_ref = r
                    logits = _dot(lr_ref[...], wgk_ref[...]) + bgk_ref[...]
                    g_all = _log_sigmoid(logits) * (1.0 / GATE_NORM)
                for h in range(NH):
                    sl = slice(h * HD, (h + 1) * HD)
                    if hg:
                        q, k, g, _, _, _ = _hg_inputs(a_ref[:, sl], c_ref[:, sl], lb_ref[...], di, sl)
                        v = b_ref[:, sl]
                    else:
                        q, k, v, g = a_ref[:, sl] * (HD ** -0.5), b_ref[:, sl], c_ref[:, sl], g_all[:, sl]
                    lanes.append((q, k, v, g, state[2 * bi + di, h], fw))
                    where.append((2 * bi + di, h, sl, o_ref, st_ref))
        qs, ks, vs, gs, st0s, fws = (list(col) for col in zip(*lanes))
        os_, st1s = _chunks_fwd(qs, ks, vs, gs, st0s, fws)
        for (si, h, sl, o_ref, st_ref), st0, o, st1 in zip(where, st0s, os_, st1s):
            st_ref[0, h] = st0
            o_ref[:, sl] = o
            state[si, h] = st1

    fixed = lambda j: (0, 0)
    in_specs, args, out_specs = [], [], []
    for branch, side in branch_sides:
        for di, fw in enumerate((True, False)):
            chunk = functools.partial(_chunk_index, n_ctx_chunks=n_ctx_chunks, n_chunks=n_chunks, fw=fw)

            def cmap(blk, width=HW, chunk=chunk):
                return pl.BlockSpec((CHUNK, width), lambda j: (chunk(j), blk))

            if branch == "hg":
                in_specs += [cmap(C_HQ), cmap(C_HI), cmap(C_HF_FW + di), pl.BlockSpec((4, HW), fixed)]
                args += [p, p, p, side]
            else:
                in_specs += [cmap(C_GQ), cmap(C_GK), cmap(C_GV), cmap(OFF_LR // 128, 128),
                             pl.BlockSpec((128, HW), fixed), pl.BlockSpec((1, HW), fixed)]
                args += [p, p, p, p, side[di][0], side[di][1]]
            out_specs += [cmap(0), pl.BlockSpec((1, NH, HD, HD), lambda j, chunk=chunk: (chunk(j), 0, 0, 0))]
    return pl.pallas_call(
        body, name=name, grid=(n_chunks,),
        out_shape=[jax.ShapeDtypeStruct((rows, HW), F32),
                   jax.ShapeDtypeStruct((n_chunks, NH, HD, HD), F32)] * (2 * n_br),
        in_specs=in_specs, out_specs=out_specs,
        scratch_shapes=[pltpu.VMEM((2 * n_br, NH, HD, HD), F32)],
        compiler_params=_cparams(dimension_semantics=("arbitrary",)),
    )(*args)


def _scan_bwd_both(p, branch_items, n_ctx_chunks, name):
    rows = p.shape[0]
    n_chunks = rows // CHUNK
    n_ins = [6 if item[0] == "hg" else 8 for item in branch_items]
    n_outs = [4 if item[0] == "hg" else 6 for item in branch_items]
    n_in_all, n_out_all = 2 * sum(n_ins), 2 * sum(n_outs)

    def body(*refs):
        ins, outs, dstate = refs[:n_in_all], refs[n_in_all:n_in_all + n_out_all], refs[-1]
        first = pl.program_id(0) == 0

        @pl.when(first)
        def _():
            dstate[...] = jnp.zeros_like(dstate)

        lanes, where, extra, ctx = [], [], [], []
        ipos = opos = 0
        for bi, item in enumerate(branch_items):
            hg = item[0] == "hg"
            for di, fw in enumerate((True, False)):
                r, w = ins[ipos:ipos + n_ins[bi]], outs[opos:opos + n_outs[bi]]
                ipos += n_ins[bi]
                opos += n_outs[bi]
                if hg:
                    a_ref, b_ref, c_ref, lb_ref, st_ref, do_ref = r
                    acc_refs = (w[3],)
                else:
                    a_ref, b_ref, c_ref, lr_ref, wgk_ref, bgk_ref, st_ref, do_ref = r
                    acc_refs = (w[4], w[5])
                    lr = lr_ref[...]
                    logits = _dot(lr, wgk_ref[...]) + bgk_ref[...]
                    g_all = _log_sigmoid(logits) * (1.0 / GATE_NORM)

                @pl.when(first)
                def _(acc_refs=acc_refs):
                    for ref in acc_refs:
                        ref[...] = jnp.zeros_like(ref)

                for h in range(NH):
                    sl = slice(h * HD, (h + 1) * HD)
                    if hg:
                        hq, hf = a_ref[:, sl], c_ref[:, sl]
                        q, k, g, f, sg, lb = _hg_inputs(hq, hf, lb_ref[...], di, sl)
                        v = b_ref[:, sl]
                        extra.append((hq, f, sg, lb))
                    else:
                        q, k, v, g = a_ref[:, sl] * (HD ** -0.5), b_ref[:, sl], c_ref[:, sl], g_all[:, sl]
                        extra.append(None)
                    lanes.append((q, k, v, g, st_ref[0, h], do_ref[:, sl], dstate[2 * bi + di, h], fw))
                    where.append((2 * bi + di, h, sl))
                ctx.append((hg, w, None if hg else (lr, logits, wgk_ref)))

        dqs, dks, dvs, dgs, dst0s = [], [], [], [], []
        for lo in range(0, len(lanes), BWD_LANES):
            cols = [list(col) for col in zip(*lanes[lo:lo + BWD_LANES])]
            for acc, part in zip((dqs, dks, dvs, dgs, dst0s), _chunks_bwd(*cols)):
                acc.extend(part)
        dg_parts = [[] for _ in ctx]
        for (si, h, sl), ex, dq, dk, dv, dg, dst0 in zip(where, extra, dqs, dks, dvs, dgs, dst0s):
            dstate[si, h] = dst0
            hg, w, _ = ctx[si]
            if hg:
                hq, f, sg, lb = ex
                da_ref, db_ref, dc_ref, dlb_ref = w
                da_ref[:, sl] = (dq * _dsilu(hq)).astype(BF16)
                db_ref[:, sl] = dv.astype(BF16)
                df = dg / f - dk
                dc_ref[:, sl] = (df * (1.0 - lb) * sg * (1.0 - sg)).astype(BF16)
                dlb_ref[0:1, sl] += _colsum(df * (1.0 - sg))
            else:
                da_ref, db_ref, dc_ref = w[:3]
                da_ref[:, sl] = (dq * (HD ** -0.5)).astype(BF16)
                db_ref[:, sl] = dk.astype(BF16)
                dc_ref[:, sl] = dv.astype(BF16)
                dg_parts[si].append(dg)
        for si, (hg, w, more) in enumerate(ctx):
            if not hg:
                dlr_ref, dwgk_ref, dbias_ref = w[3:]
                lr, logits, wgk_ref = more
                dlogits = jnp.concatenate(dg_parts[si], axis=1) * (1.0 / GATE_NORM) * (1.0 - _sigmoid(logits))
                dlr_ref[...] = _dot_nt(dlogits, wgk_ref[...]).astype(BF16)
                dwgk_ref[...] += _dot_tn(lr, dlogits)
                dbias_ref[0:1, :] += _colsum(dlogits)

    fixed = lambda j: (0, 0)
    big = jax.ShapeDtypeStruct((rows, HW), BF16)
    in_specs, args, out_shape, out_specs = [], [], [], []
    for branch, side, states, d_o in branch_items:
        for di, fw in enumerate((True, False)):
            def chunk_of(j, fw=fw):
                return _chunk_index(n_chunks - 1 - j, n_ctx_chunks, n_chunks, fw)

            def cmap(blk, width=HW, chunk_of=chunk_of):
                return pl.BlockSpec((CHUNK, width), lambda j: (chunk_of(j), blk))

            st_spec = pl.BlockSpec((1, NH, HD, HD), lambda j, chunk_of=chunk_of: (chunk_of(j), 0, 0, 0))
            if branch == "hg":
                in_specs += [cmap(C_HQ), cmap(C_HI), cmap(C_HF_FW + di), pl.BlockSpec((4, HW), fixed), st_spec,
                             cmap(0)]
                args += [p, p, p, side, states[di], d_o]
                out_shape += [big, big, big, jax.ShapeDtypeStruct((8, HW), F32)]
                out_specs += [cmap(0), cmap(0), cmap(0), pl.BlockSpec((8, HW), fixed)]
            else:
                in_specs += [cmap(C_GQ), cmap(C_GK), cmap(C_GV), cmap(OFF_LR // 128, 128),
                             pl.BlockSpec((128, HW), fixed), pl.BlockSpec((1, HW), fixed), st_spec, cmap(0)]
                args += [p, p, p, p, side[di][0], side[di][1], states[di], d_o]
                out_shape += [big, big, big, jax.ShapeDtypeStruct((rows, 128), BF16),
                              jax.ShapeDtypeStruct((128, HW), F32), jax.ShapeDtypeStruct((8, HW), F32)]
                out_specs += [cmap(0), cmap(0), cmap(0), cmap(0, 128), pl.BlockSpec((128, HW), fixed),
                              pl.BlockSpec((8, HW), fixed)]
    return pl.pallas_call(
        body, name=name, grid=(n_chunks,), out_shape=out_shape, in_specs=in_specs, out_specs=out_specs,
        scratch_shapes=[pltpu.VMEM((2 * len(branch_items), NH, HD, HD), F32)],
        compiler_params=_cparams(dimension_semantics=("arbitrary",)),
    )(*args)


SMALL_ROWS = 56
ROWS_MOD_X = (0, 1, 8, 16, 17, 18)
ROWS_MOD_C = (2, 3)
ROW_PRE1, ROW_POST1, ROW_ONORM, ROW_PRE2, ROW_POST2, ROW_LB, ROW_BGK, ROW_WGK = 4, 9, 10, 19, 20, 24, 32, 40
ROW_LOSS = 21


def _reduce_small(gathered, lb_full, name, after=()):
    _, _, d = gathered.shape

    def body(g_ref, lb_ref, *rest):
        sum_ref, dmod_ref, dbmod_ref, dlb_ref = rest[len(after):]
        total = g_ref[0]
        for b in range(1, N_DEV):
            total = total + g_ref[b]
        sum_ref[...] = total
        dmod_ref[...] = jnp.zeros_like(dmod_ref)
        for m in range(N_MOD):
            col = slice(m * d, (m + 1) * d)
            acc = jnp.zeros((1, d), F32)
            for b in range(N_DEV):
                row = g_ref[b, ROWS_MOD_X[m]:ROWS_MOD_X[m] + 1, :]
                dmod_ref[b:b + 1, col] = row
                acc = acc + row
            if m < 2:
                ctx_row = total[ROWS_MOD_C[m]:ROWS_MOD_C[m] + 1, :]
                dmod_ref[8:9, col] = ctx_row
                acc = acc + ctx_row
            dbmod_ref[:, col] = acc
        lbv = lb_ref[...]
        for dd in range(2):
            lb = _sigmoid(lbv[dd:dd + 1, :] - lbv[2 + dd:3 + dd, :])
            gl = total[ROW_LB:ROW_LB + 1, dd * HW:(dd + 1) * HW] * lb * (1.0 - lb)
            dlb_ref[dd:dd + 1, :] = gl
            dlb_ref[2 + dd:3 + dd, :] = -gl

    return pl.pallas_call(
        body, name=name,
        out_shape=[jax.ShapeDtypeStruct((SMALL_ROWS, d), F32), jax.ShapeDtypeStruct((16, N_MOD * d), F32),
                   jax.ShapeDtypeStruct((1, N_MOD * d), F32), jax.ShapeDtypeStruct((4, HW), F32)],
        in_specs=[VMEM_SPEC] * 2 + [ANY_SPEC] * len(after), out_specs=[VMEM_SPEC] * 4, compiler_params=_cparams(),
    )(gathered, lb_full, *after)


def _c_ctx_grad(gathered, c_ctx_row, name):
    def body(g_ref, c_ref, o_ref):
        acc = g_ref[0, 0:1, :]
        for chip in range(1, N_CHIP):
            acc = acc + g_ref[2 * chip, 0:1, :]
        o_ref[...] = acc * _dsilu(c_ref[...])

    return pl.pallas_call(
        body, name=name, out_shape=jax.ShapeDtypeStruct(c_ctx_row.shape, F32),
        in_specs=[VMEM_SPEC] * 2, out_specs=VMEM_SPEC, compiler_params=_cparams(),
    )(gathered, c_ctx_row)


def _blocked(full, n_blocks):
    k, n = full.shape
    return full.reshape(k, n_blocks, n // n_blocks).transpose(1, 0, 2)


def _unblocked(blocks):
    nb, k, n = blocks.shape
    return blocks.transpose(1, 0, 2).reshape(k, nb * n)


def _sample_front(x0, ctx0, modc, modx, norm_pre1, lb_full, gla_side, w_in_r):
    ctx_len = ctx0.shape[0]
    n_ctx_tiles = ctx_len // TM
    n_ctx_chunks = ctx_len // CHUNK
    h1, p = _in_projection(ctx0, x0, modc, modx, norm_pre1, w_in_r, n_ctx_tiles, "in_projection")
    (o_hg_fw, st_hg_fw, o_hg_bw, st_hg_bw, o_gla_fw, st_gla_fw, o_gla_bw, st_gla_bw) = _scan_fwd_both(
        p, [("hg", lb_full), ("gla", gla_side)], n_ctx_chunks, "scan_fwd")
    return dict(h1=h1, p=p, o_list=[o_hg_fw, o_hg_bw, o_gla_fw, o_gla_bw],
                states=[st_hg_fw, st_hg_bw, st_gla_fw, st_gla_bw])


def _sample_back(reduce, front, x0, ctx0, target0, modc, modx, norm_pre1, norms, onorms, lb_full, gla_side, w_in_r,
                 wbh, wbg, wout, ffn_weights):
    seq, d = x0.shape
    ctx_len = ctx0.shape[0]
    n_ctx_tiles = ctx_len // TM
    n_tiles = (ctx_len + seq) // TM
    n_ctx_chunks = ctx_len // CHUNK
    h1, p, o_list = front["h1"], front["p"], front["o_list"]
    st_hg_fw, st_hg_bw, st_gla_fw, st_gla_bw = front["states"]
    z2, y1, merged, og_hg, og_gla = _mixer_tail_fwd(x0, p, o_list, modx, norms, onorms, wbh, wbg, wout, n_ctx_tiles,
                                                    "mixer_tail")
    wg, wu, wd = ffn_weights([z2])
    loss_part, dz2, h2, a_act, du, dv, dy2, stat_ffn = _ffn_fwd_bwd(z2, modx, norms, wg, wu, wd, target0, "ffn")
    dff = wg.shape[0]
    tok = reduce("ffn", [_weight_grad(du, h2, "grad_w_ff_gate", tk=dff // 2, tn=d),
                         _weight_grad(dv, h2, "grad_w_ff_up", tk=dff // 2, tn=d),
                         _weight_grad(a_act, dy2, "grad_w_ff_down", tk=dff // 2)])

    (d_ohg, d_ogla, d_hgate, d_ggate, d_ghg, d_ggla, dy1, db_hg, db_gla, stat_mix) = _mixer_tail_bwd(
        x0, p, o_list, dz2, y1, modx + tok, norms, onorms, wbh, wbg, wout, n_ctx_tiles, n_tiles, "mixer_tail_bwd")
    tok = reduce("mix", [_weight_grad(og_hg, db_hg, "grad_w_br_hg"), _weight_grad(og_gla, db_gla, "grad_w_br_gla"),
                         _weight_grad(merged, dy1, "grad_w_out")])
    tok = tok + reduce("push_ffn", [dy1])
    gla_b = [(wgk, bias + tok) for wgk, bias in gla_side]
    (dgq_f, dgk_f, dgv_f, dlr_f, dwgk_f, dbgk_f, dgq_b, dgk_b, dgv_b, dlr_b, dwgk_b, dbgk_b,
     dhq_f, dhi_f, dhf_f, dlb_f, dhq_b, dhi_b, dhf_b, dlb_b) = _scan_bwd_both(
        p, [("gla", gla_b, (st_gla_fw, st_gla_bw), d_ogla), ("hg", lb_full, (st_hg_fw, st_hg_bw), d_ohg)],
        n_ctx_chunks, "scan_bwd")
    tok = reduce("push_mix", [dbgk_f])
    pieces = [dhq_f, dhq_b, dhi_f, dhi_b, dhf_f, dhf_b, d_hgate, dgq_f, dgq_b, dgk_f, dgk_b, dgv_f, dgv_b, d_ggate,
              d_ghg, d_ggla, dlr_f, dlr_b]
    dp, grad_x, stat_in = _in_projection_bwd(ctx0, x0, dz2, modc, modx, norm_pre1 + tok, w_in_r, pieces, n_ctx_tiles,
                                             "in_projection_bwd")

    started = reduce("small_start", dict(stat_in=stat_in, stat_mix=stat_mix, stat_ffn=stat_ffn, dlb=(dlb_f, dlb_b),
                                         dwgk=(dwgk_f, dwgk_b), dbgk=(dbgk_f, dbgk_b)))
    reduce("in", [_w_in_grad(dp, h1, w_in_r.shape[0], "grad_w_in", after=[started])])
    reduce("small", None)
    reduce("push_in", [])
    return dict(loss_part=loss_part, grad_x=grad_x)


def kernel(x, c, ctx, c_ctx, w_mod, b_mod, norm_pre1, norm_post1, norm_pre2, norm_post2, w_in, hg_lb, hg_onorm, gla_w_gk, gla_b_gk, gla_onorm, w_br_hg, w_br_gla, w_out, w_ff_gate, w_ff_up, w_ff_down, loss_target, m_c_ctx, m_w_mod, m_b_mod, m_norm_pre1, m_norm_post1, m_norm_pre2, m_norm_post2, m_w_in, m_hg_lb, m_hg_onorm, m_gla_w_gk, m_gla_b_gk, m_gla_onorm, m_w_br_hg, m_w_br_gla, m_w_out, m_w_ff_gate, m_w_ff_up, m_w_ff_down, v_c_ctx, v_w_mod, v_b_mod, v_norm_pre1, v_norm_post1, v_norm_pre2, v_norm_post2, v_w_in, v_hg_lb, v_hg_onorm, v_gla_w_gk, v_gla_b_gk, v_gla_onorm, v_w_br_hg, v_w_br_gla, v_w_out, v_w_ff_gate, v_w_ff_up, v_w_ff_down):
    seq, d = x.shape[1], x.shape[2]
    ctx_len = ctx.shape[1]
    assert seq % TM == 0 and ctx_len % TM == 0 and d == 2 * HW
    ax, ay, ac = lax.axis_index("x"), lax.axis_index("y"), lax.axis_index("c")
    chip = 2 * ax + ay
    dev = 2 * chip + ac
    c_arr = jnp.reshape(ac, (1,)).astype(jnp.int32)
    chip_arr = jnp.reshape(chip, (1,)).astype(jnp.int32)
    transposed = ("w_in", "w_ff_gate", "w_ff_up")
    view = lambda a, nm: a[0].T if nm in transposed else a[0]

    sems_in, lands_in, token_in0 = _blocks_start([_cast_into_blocks(chip_arr, view(w_in, "w_in"), "cast_w_in")],
                                                 "gather_w_in_start")

    nc = d // 128
    pad8 = lambda a: jnp.pad(a, ((0, -a.shape[0] % 8), (0, 0)))
    small1 = jnp.concatenate([c.reshape(nc, 128) + token_in0[0, 0], pad8(hg_lb.reshape(4, 128)),
                              gla_w_gk.reshape(2 * RANK, 128), pad8(gla_b_gk.reshape(2, 128))], axis=0)
    blocks = [_cast_into_blocks(chip_arr, view(w_, nm), "cast_" + nm) for w_, nm in (
        (w_br_hg, "w_br_hg"), (w_br_gla, "w_br_gla"), (w_out, "w_out"), (w_ff_gate, "w_ff_gate"),
        (w_ff_up, "w_ff_up"), (w_ff_down, "w_ff_down"))]
    got1 = _allgather8(small1, "gather_small_params", after=blocks)
    c_all = got1[:, :nc, :].reshape(N_DEV, d)
    per_chip = got1[0::2]
    lb_full = per_chip[:, nc:nc + 4, :].transpose(1, 0, 2).reshape(4, HW)
    wgk_full = per_chip[:, nc + 8:nc + 8 + 2 * RANK, :].transpose(1, 0, 2).reshape(2, RANK, HW)
    bgk_full = per_chip[:, nc + 8 + 2 * RANK:nc + 10 + 2 * RANK, :].transpose(1, 0, 2).reshape(2, HW)
    wgk_pad = [jnp.zeros((128, HW), F32).at[dd * RANK:(dd + 1) * RANK].set(wgk_full[dd]) for dd in range(2)]
    bgk = [bgk_full[dd:dd + 1] for dd in range(2)]

    n_mod_cols = w_mod.shape[2]
    cond = jnp.concatenate([c_all, pad8(c_ctx.reshape(1, d))], axis=0)
    b_cols = lax.dynamic_slice(b_mod, (0, chip * n_mod_cols), (1, n_mod_cols))
    lands_in = _blocks_wait(sems_in, lands_in, [got1], "gather_w_in_wait")
    fwd_sems, lands_in, fwd_token = _forward_start(lands_in, "gather_w_in_forward_start")
    mod_part = _mod_forward(cond + fwd_token[0, 0], w_mod[0], b_cols, "mod_forward")
    mod_got = _allgather8(mod_part, "gather_mod")
    mod_all = mod_got[0::2].transpose(1, 0, 2).reshape(16, N_CHIP * n_mod_cols)
    modx = pad8(lax.dynamic_slice(mod_all, (dev, 0), (1, N_MOD * d)).reshape(N_MOD, d))
    modc = pad8(mod_all[8].reshape(N_MOD, d))

    gathered_in = _forward_wait(fwd_sems, lands_in, [mod_got], "gather_w_in_forward_wait")
    sems, lands, token = _blocks_start(blocks, "gather_rest_start", after=[gathered_in[0]])
    w_in_r = gathered_in[0].reshape(-1, d)

    norms = jnp.concatenate([norm_pre1, norm_post1, norm_pre2, norm_post2, jnp.zeros((4, d), F32)], axis=0)
    onorms = jnp.zeros((8, d), F32).at[0, :HD].set(hg_onorm[0]).at[1, :HD].set(gla_onorm[0])
    gla_side = [(wgk_pad[dd], bgk[dd]) for dd in range(2)]
    modx = modx + token[0, 0]
    front = _sample_front(x[0], ctx[0], modc, modx, norm_pre1, lb_full, gla_side, w_in_r)
    lands = _blocks_wait(sems, lands, front["o_list"], "gather_rest_wait")
    gathered = _blocks_finish(lands[:3], "gather_mix_finish")
    wbh, wbg = _unblocked(gathered[0]), _unblocked(gathered[1])
    wout = gathered[2].reshape(d, d)
    ffn_sems, ffn_lands, ffn_token = _forward_start(lands[3:], "gather_ffn_forward_start")
    onorms = onorms + ffn_token[0, 0]

    def ffn_weights(after):
        got = _forward_wait(ffn_sems, ffn_lands, after, "gather_ffn_forward_wait")
        return tuple(g.reshape(-1, d) for g in got)

    dff = w_ff_down.shape[1] * N_CHIP
    groups = {"ffn": ["w_ff_gate", "w_ff_up", "w_ff_down"], "mix": ["w_br_hg", "w_br_gla", "w_out"], "in": ["w_in"]}
    row_sharded = {"w_out": d // N_CHIP, "w_ff_down": dff // N_CHIP, "w_ff_gate": dff // N_CHIP,
                   "w_ff_up": dff // N_CHIP, "w_in": w_in.shape[2]}
    in_flight, to_sibling, small = {}, {}, {}

    def reduce_small_start(stats):
        small2 = jnp.concatenate([
            stats["stat_in"], stats["stat_mix"], stats["stat_ffn"],
            jnp.concatenate(stats["dlb"], axis=1), jnp.concatenate(stats["dbgk"], axis=1),
            jnp.concatenate([stats["dwgk"][0][0:RANK], stats["dwgk"][1][RANK:2 * RANK]], axis=1)], axis=0)
        assert small2.shape[0] == SMALL_ROWS
        sems_, land_, token_ = _allgather8_start(small2, "gather_small_grads_start")
        small.update(gathering=(sems_, land_))
        return token_

    def reduce_small():
        got2 = small["gathered"]
        total, dmod_all, g_b_mod, g_lb_full = _reduce_small(got2, lb_full, "reduce_small", to_sibling["in"][1])
        dmod_cols = lax.dynamic_slice(dmod_all, (0, chip * n_mod_cols), (16, n_mod_cols))
        g_w_mod, cctx_part = _mod_backward(cond, w_mod[0], dmod_cols, "mod_backward")
        got3 = _allgather8(cctx_part, "gather_c_ctx_grad")
        g_c_ctx = _c_ctx_grad(got3, c_ctx.reshape(1, d), "c_ctx_grad")
        small.update(total=total, g_b_mod=g_b_mod, g_lb_full=g_lb_full, g_w_mod=g_w_mod, g_c_ctx=g_c_ctx)

    def reduce(group, grads):
        if group == "small_start":
            return reduce_small_start(grads)
        if group == "small":
            return reduce_small()
        if group.startswith("push_"):
            return push(group[5:], grads)
        nms = groups[group]
        after = []
        if group == "in":
            small.update(gathered=_allgather8_wait(*small["gathering"], grads, "gather_small_grads_wait"))
            after = [small["gathered"]]
        full = [g.reshape(N_CHIP, row_sharded[nm], d) if nm in row_sharded else _blocked(g, N_CHIP)
                for g, nm in zip(grads, nms)]
        sems_, full, lands_, token_ = _send_half_start(full, "grads_to_sibling_start_" + group, after)
        to_sibling[group] = (sems_, full, lands_)
        return token_[0, 0]

    def push(group, after):
        nms = groups[group]
        sems_, full, lands_ = to_sibling[group]
        if group == "in":
            after = list(after) + [small["g_c_ctx"], small["total"]]
        full, from_sibling = _send_half_wait(sems_, full, lands_, after, "grads_to_sibling_wait_" + group)
        pairs = [_pair_sum(c_arr, f, r_, "pair_sum_" + nm) for f, r_, nm in zip(full, from_sibling, nms)]
        after = [small["g_c_ctx"], small["total"]] if group == "in" else []
        sems_, pairs, lands_, token_ = _scatter_start(pairs, "grads_to_owner_start_" + group, after)
        in_flight[group] = (sems_, pairs, lands_, token_)
        return token_[0, 0]

    r = _sample_back(reduce, front, x[0], ctx[0], loss_target[0], modc, modx, norm_pre1, norms, onorms, lb_full,
                     gla_side, w_in_r, wbh, wbg, wout, ffn_weights)
    grad_x = r["grad_x"]

    weights = dict(w_in=(w_in, m_w_in, v_w_in), w_br_hg=(w_br_hg, m_w_br_hg, v_w_br_hg),
                   w_br_gla=(w_br_gla, m_w_br_gla, v_w_br_gla), w_out=(w_out, m_w_out, v_w_out),
                   w_ff_gate=(w_ff_gate, m_w_ff_gate, v_w_ff_gate), w_ff_up=(w_ff_up, m_w_ff_up, v_w_ff_up),
                   w_ff_down=(w_ff_down, m_w_ff_down, v_w_ff_down))
    names = ["w_in", "w_br_hg", "w_br_gla", "w_out", "w_ff_gate", "w_ff_up", "w_ff_down"]
    big, swapping = {}, {}

    def sum_and_swap(group, after):
        sems_, pairs, lands_, _ = in_flight[group]
        pairs, lands_ = _scatter_wait(sems_, pairs, lands_, after, "grads_to_owner_wait_" + group)
        own_half = [_sum_owner(chip_arr, pr, g, "chip_sum_" + nm) for pr, g, nm in zip(pairs, lands_, groups[group])]
        swapping[group] = _swap_start(own_half, "halves_to_sibling_start_" + group)
        return own_half[-1]

    def update(group, after):
        sems_, own_half, lands_ = swapping[group]
        own_half, other_half = _swap_wait(sems_, own_half, lands_, after, "halves_to_sibling_wait_" + group)
        done = []
        for nm, own, oth in zip(groups[group], own_half, other_half):
            w_, m_, v_ = (view(a, nm) for a in weights[nm])
            res = _adamw_halves(c_arr, own, oth, w_, m_, v_, "adamw_" + nm)
            big[nm] = [r_.T[None] if nm in transposed else r_[None] for r_ in res]
            done.append(res[1])
        return done

    token_in = in_flight["in"][3]
    summed_ffn = sum_and_swap("ffn", [token_in])
    summed_mix = sum_and_swap("mix", [summed_ffn])

    total, g_b_mod, g_lb_full, g_w_mod, g_c_ctx = (small[k] for k in ("total", "g_b_mod", "g_lb_full", "g_w_mod",
                                                                      "g_c_ctx"))
    g_pre1, g_post1, g_pre2, g_post2 = (total[r_:r_ + 1] for r_ in (ROW_PRE1, ROW_POST1, ROW_PRE2, ROW_POST2))
    g_hg_on, g_gla_on = total[ROW_ONORM:ROW_ONORM + 1, 0:HD], total[ROW_ONORM:ROW_ONORM + 1, HD:2 * HD]
    n_lb = hg_lb.shape[2]
    g_hg_lb = lax.dynamic_slice(g_lb_full, (0, chip * n_lb), (4, n_lb))
    g_bgk = lax.dynamic_slice(total[ROW_BGK:ROW_BGK + 1].reshape(2, HW), (0, chip * n_lb), (2, n_lb))
    g_wgk_full = total[ROW_WGK:ROW_WGK + RANK].reshape(RANK, 2, HW).transpose(1, 0, 2).reshape(2 * RANK, HW)
    g_wgk = lax.dynamic_slice(g_wgk_full, (0, chip * n_lb), (2 * RANK, n_lb))

    small_items = [
        (g_c_ctx, c_ctx.reshape(1, d), m_c_ctx.reshape(1, d), v_c_ctx.reshape(1, d)),
        (g_b_mod, b_mod, m_b_mod, v_b_mod),
        (g_pre1, norm_pre1, m_norm_pre1, v_norm_pre1),
        (g_post1, norm_post1, m_norm_post1, v_norm_post1),
        (g_pre2, norm_pre2, m_norm_pre2, v_norm_pre2),
        (g_post2, norm_post2, m_norm_post2, v_norm_post2),
        (g_hg_lb, hg_lb.reshape(4, n_lb), m_hg_lb.reshape(4, n_lb), v_hg_lb.reshape(4, n_lb)),
        (g_hg_on, hg_onorm, m_hg_onorm, v_hg_onorm),
        (g_wgk, gla_w_gk.reshape(2 * RANK, n_lb), m_gla_w_gk.reshape(2 * RANK, n_lb), v_gla_w_gk.reshape(2 * RANK, n_lb)),
        (g_bgk, gla_b_gk.reshape(2, n_lb), m_gla_b_gk.reshape(2, n_lb), v_gla_b_gk.reshape(2, n_lb)),
        (g_gla_on, gla_onorm, m_gla_onorm, v_gla_onorm),
    ]
    small_res = _adamw_whole(small_items, "adamw_small")
    mod_res = _adamw_tiled(g_w_mod, w_mod[0], m_w_mod[0], v_w_mod[0], "adamw_w_mod")
    done_ffn = update("ffn", [summed_mix, mod_res[0], small_res[0][0]])
    done_mix = update("mix", done_ffn)
    update("in", [sum_and_swap("in", done_mix)])

    loss = total[ROW_LOSS, 0]

    shapes = dict(c_ctx=c_ctx.shape, b_mod=b_mod.shape, norm_pre1=norm_pre1.shape, norm_post1=norm_post1.shape,
                  norm_pre2=norm_pre2.shape, norm_post2=norm_post2.shape, hg_lb=hg_lb.shape, hg_onorm=hg_onorm.shape,
                  gla_w_gk=gla_w_gk.shape, gla_b_gk=gla_b_gk.shape, gla_onorm=gla_onorm.shape)
    small_names = ["c_ctx", "b_mod", "norm_pre1", "norm_post1", "norm_pre2", "norm_post2", "hg_lb", "hg_onorm",
                   "gla_w_gk", "gla_b_gk", "gla_onorm"]
    grads, deltas, new_m, new_v = {}, {}, {}, {}
    for nm, item, res in zip(small_names, small_items, small_res):
        grads[nm] = item[0].reshape(shapes[nm])
        deltas[nm], new_m[nm], new_v[nm] = (r_.reshape(shapes[nm]) for r_ in res)
    grads["w_mod"] = g_w_mod[None]
    deltas["w_mod"], new_m["w_mod"], new_v["w_mod"] = (r_[None] for r_ in mod_res)
    for nm in names:
        grads[nm], deltas[nm], new_m[nm], new_v[nm] = big[nm]
    order = ["c_ctx", "w_mod", "b_mod", "norm_pre1", "norm_post1", "norm_pre2", "norm_post2", "w_in", "hg_lb",
             "hg_onorm", "gla_w_gk", "gla_b_gk", "gla_onorm", "w_br_hg", "w_br_gla", "w_out", "w_ff_gate", "w_ff_up",
             "w_ff_down"]
    return (loss, grad_x[None], *[grads[n] for n in order], *[deltas[n] for n in order],
            *[new_m[n] for n in order], *[new_v[n] for n in order])
```

```python
import functools

import jax
import jax.numpy as jnp
from jax import lax
from jax.experimental import pallas as pl
from jax.experimental.pallas import tpu as pltpu

F32 = jnp.float32
BF16 = jnp.bfloat16
MESH = pl.DeviceIdType.MESH

EPS = 1e-6
CHUNK = 64
SUB = 16
NSUB = CHUNK // SUB
NH = 4
HD = 128
HW = NH * HD
RANK = 16
GATE_NORM = 16.0
N_MOD = 6
TM = 256
BWD_LANES = 8
N_DEV = 8
N_CHIP = 4
VMEM_LIMIT = 56 * 1024 * 1024

ADAM_LR = 0.001
ADAM_B1 = 0.9
ADAM_B2 = 0.999
ADAM_EPS = 1e-08
ADAM_WD = 0.01
ADAM_STEP = 10

VMEM_SPEC = pl.BlockSpec(memory_space=pltpu.VMEM)
ANY_SPEC = pl.BlockSpec(memory_space=pl.ANY)
HBM_SPEC = pl.BlockSpec(memory_space=pltpu.HBM)
SEM_SPEC = pl.BlockSpec(memory_space=pltpu.SEMAPHORE)
EFFECT = pltpu.SideEffectType.DATAFLOW_SIDE_EFFECTING


def _cparams(**kw):
    return pltpu.CompilerParams(vmem_limit_bytes=VMEM_LIMIT, **kw)


def _dot(a, b):
    return jnp.dot(a.astype(BF16), b.astype(BF16), preferred_element_type=F32)


def _dot_nt(a, b):
    return lax.dot_general(a.astype(BF16), b.astype(BF16), (((1,), (1,)), ((), ())), preferred_element_type=F32)


def _dot_tn(a, b):
    return lax.dot_general(a.astype(BF16), b.astype(BF16), (((0,), (0,)), ((), ())), preferred_element_type=F32)


def _sigmoid(x):
    return 1.0 / (1.0 + jnp.exp(-x))


def _silu(x):
    return x * _sigmoid(x)


def _dsilu(x):
    s = _sigmoid(x)
    return s * (1.0 + x * (1.0 - s))


def _log_sigmoid(x):
    return jnp.minimum(x, 0.0) - jnp.log(1.0 + jnp.exp(-jnp.abs(x)))


def _colsum(a):
    return jnp.sum(a, axis=0, keepdims=True)


def _rms(a):
    r = lax.rsqrt(jnp.mean(a * a, axis=-1, keepdims=True) + EPS)
    return a * r, r


def _rms_bwd(dn, n, r):
    return r * (dn - n * jnp.mean(dn * n, axis=-1, keepdims=True))


def _place():
    x, y, c = lax.axis_index("x"), lax.axis_index("y"), lax.axis_index("c")
    chips = [(1 - x, y), (x, 1 - y), (1 - x, 1 - y)]
    return x, y, c, chips


def _allgather8(v, name, after=()):
    rows, cols = v.shape
    n_after = len(after)

    def body(x_ref, *rest):
        out_ref, send_sems, recv_sems, local_sem = rest[n_after:]
        x, y, c, chips = _place()
        me, sibling = (x, y, c), (x, y, 1 - c)

        def blk(px, py, pc):
            return out_ref.at[4 * px + 2 * py + pc]

        def copy(k, block, to, src=None):
            return pltpu.make_async_remote_copy(
                src_ref=blk(*block) if src is None else src, dst_ref=blk(*block),
                send_sem=send_sems.at[k], recv_sem=recv_sems.at[k], device_id=to, device_id_type=MESH)

        mine = pltpu.make_async_copy(x_ref, blk(*me), local_sem)
        mine.start()
        first = [copy(0, me, sibling, src=x_ref)]
        first += [copy(1 + j, me, (*chip, c), src=x_ref) for j, chip in enumerate(chips)]
        for cp in first:
            cp.start()
        passed = [copy(4 + j, (*chip, c), sibling) for j, chip in enumerate(chips)]
        for j, chip in enumerate(chips):
            copy(1 + j, (*chip, c), me).wait_recv()
            passed[j].start()
        copy(0, sibling, me).wait_recv()
        for j, chip in enumerate(chips):
            copy(4 + j, (*chip, 1 - c), me).wait_recv()
        for cp in first + passed:
            cp.wait_send()
        mine.wait()

    return pl.pallas_call(
        body, name=name,
        out_shape=jax.ShapeDtypeStruct((N_DEV, rows, cols), v.dtype),
        in_specs=[VMEM_SPEC] + [ANY_SPEC] * n_after, out_specs=VMEM_SPEC,
        scratch_shapes=[pltpu.SemaphoreType.DMA((7,)), pltpu.SemaphoreType.DMA((7,)), pltpu.SemaphoreType.DMA],
    )(v, *after)


def _allgather8_start(v, name):
    rows, cols = v.shape

    def body(x_ref, out_ref, *rest):
        send_sems, recv_sems, token, local_sem = rest[:4], rest[4:8], rest[-2], rest[-1]
        x, y, c, chips = _place()
        own = out_ref.at[4 * x + 2 * y + c]
        mine = pltpu.make_async_copy(x_ref, own, local_sem)
        mine.start()
        mine.wait()
        for k, to in enumerate([(x, y, 1 - c)] + [(*chip, c) for chip in chips]):
            pltpu.make_async_remote_copy(src_ref=own, dst_ref=own, send_sem=send_sems[k], recv_sem=recv_sems[k],
                                         device_id=to, device_id_type=MESH).start()
        token[...] = jnp.zeros_like(token)

    land = _hbm(lax.empty((N_DEV, rows, cols), v.dtype))
    out = pl.pallas_call(
        body, name=name,
        out_shape=(*[pltpu.SemaphoreType.DMA(())] * 8, pltpu.HBM(land.shape, land.dtype),
                   jax.ShapeDtypeStruct((8, 128), F32)),
        in_specs=[VMEM_SPEC, HBM_SPEC],
        out_specs=(*[SEM_SPEC] * 8, HBM_SPEC, VMEM_SPEC),
        input_output_aliases={1: 8},
        scratch_shapes=[pltpu.SemaphoreType.DMA],
        compiler_params=pltpu.CompilerParams(has_side_effects=EFFECT),
    )(v, land)
    return list(out[:8]), out[8], out[9]


def _allgather8_wait(sems, land, after, name):
    def blk(ref, px, py, pc):
        return ref.at[4 * px + 2 * py + pc]

    def wait_body(out_ref, *rest):
        send_sems, recv_sems = rest[:4], rest[4:8]
        x, y, c, chips = _place()
        me = (x, y, c)
        for k, peer in enumerate([(x, y, 1 - c)] + [(*chip, c) for chip in chips]):
            sent = pltpu.make_async_remote_copy(
                src_ref=blk(out_ref, *me), dst_ref=blk(out_ref, *me), send_sem=send_sems[k], recv_sem=recv_sems[k],
                device_id=peer, device_id_type=MESH)
            sent.wait_send()
            pltpu.make_async_remote_copy(
                src_ref=blk(out_ref, *peer), dst_ref=blk(out_ref, *peer), send_sem=send_sems[k],
                recv_sem=recv_sems[k], device_id=me, device_id_type=MESH).wait_recv()

    def pass_body(out_ref, _, send_sems, recv_sems):
        x, y, c, chips = _place()
        started = []
        for j, chip in enumerate(chips):
            cp = pltpu.make_async_remote_copy(
                src_ref=blk(out_ref, *chip, c), dst_ref=blk(out_ref, *chip, c), send_sem=send_sems.at[j],
                recv_sem=recv_sems.at[j], device_id=(x, y, 1 - c), device_id_type=MESH)
            cp.start()
            started.append(cp)
        for j, chip in enumerate(chips):
            pltpu.make_async_remote_copy(
                src_ref=blk(out_ref, *chip, 1 - c), dst_ref=blk(out_ref, *chip, 1 - c), send_sem=send_sems.at[j],
                recv_sem=recv_sems.at[j], device_id=(x, y, c), device_id_type=MESH).wait_recv()
        for cp in started:
            cp.wait_send()

    land = pl.pallas_call(
        wait_body, name=name,
        out_shape=pltpu.HBM(land.shape, land.dtype),
        in_specs=[HBM_SPEC] + [SEM_SPEC] * 8 + [ANY_SPEC] * len(after),
        out_specs=HBM_SPEC,
        input_output_aliases={0: 0},
        compiler_params=pltpu.CompilerParams(has_side_effects=EFFECT),
    )(land, *sems, *after)
    return pl.pallas_call(
        pass_body, name=name + "_pass",
        out_shape=jax.ShapeDtypeStruct(land.shape, land.dtype),
        in_specs=[ANY_SPEC], out_specs=ANY_SPEC,
        input_output_aliases={0: 0},
        scratch_shapes=[pltpu.SemaphoreType.DMA((3,)), pltpu.SemaphoreType.DMA((3,))],
    )(land)


def _cast_into_blocks(chip_arr, w, name):
    rows, cols = w.shape
    tr = _row_tile(rows, 16, 256)

    def body(chip_ref, w_ref, o_ref):
        o_ref[0] = w_ref[...].astype(BF16)

    return pl.pallas_call(
        body, name=name,
        grid_spec=pltpu.PrefetchScalarGridSpec(
            num_scalar_prefetch=1, grid=(rows // tr,),
            in_specs=[pl.BlockSpec((tr, cols), lambda i, chip_ref: (i, 0))],
            out_specs=pl.BlockSpec((1, tr, cols), lambda i, chip_ref: (chip_ref[0], i, 0))),
        out_shape=jax.ShapeDtypeStruct((N_CHIP, rows, cols), BF16),
        compiler_params=_cparams(dimension_semantics=("parallel",)),
    )(chip_arr, w)


def _halved_by_rows(shape):
    return (shape[1] // 2) % 16 == 0


def _half_of(ref, pc, block=None):
    lead = slice(None) if block is None else block
    if _halved_by_rows(ref.shape):
        h = ref.shape[1] // 2
        return ref.at[lead, pl.ds(pl.multiple_of(pc * h, 16), h), :]
    h = ref.shape[2] // 2
    return ref.at[lead, :, pl.ds(pl.multiple_of(pc * h, 128), h)]


def _half_shape(shape):
    return (shape[0], shape[1] // 2, shape[2]) if _halved_by_rows(shape) else (shape[0], shape[1], shape[2] // 2)


def _half_rows(ref, chip_id, pc):
    return _half_of(ref, pc, chip_id)


def _hbm(a):
    return pltpu.with_memory_space_constraint(a, pltpu.HBM)


def _blocks_start(lands, name, after=()):
    n = len(lands)
    n_sem = 3 * n
    first = n + len(after)

    def body(*refs):
        lnd = refs[:n]
        send_sems, recv_sems = refs[first:first + n_sem], refs[first + n_sem:first + 2 * n_sem]
        token = refs[-1]
        x, y, c, chips = _place()
        me_chip = 2 * x + y
        for k in range(n):
            for j, chip in enumerate(chips):
                pltpu.make_async_remote_copy(
                    src_ref=_half_rows(lnd[k], me_chip, c), dst_ref=_half_rows(lnd[k], me_chip, c),
                    send_sem=send_sems[3 * k + j], recv_sem=recv_sems[3 * k + j],
                    device_id=(*chip, c), device_id_type=MESH).start()
        token[...] = jnp.zeros_like(token)

    out = pl.pallas_call(
        body, name=name,
        out_shape=(*[pltpu.SemaphoreType.DMA(())] * (2 * n_sem),
                   *[pltpu.HBM(l.shape, l.dtype) for l in lands],
                   jax.ShapeDtypeStruct((8, 128), F32)),
        in_specs=[HBM_SPEC] * n + [ANY_SPEC] * len(after),
        out_specs=(*[SEM_SPEC] * (2 * n_sem), *[HBM_SPEC] * n, VMEM_SPEC),
        input_output_aliases={i: 2 * n_sem + i for i in range(n)},
        compiler_params=pltpu.CompilerParams(has_side_effects=EFFECT),
    )(*[_hbm(l) for l in lands], *after)
    return list(out[:2 * n_sem]), list(out[2 * n_sem:2 * n_sem + n]), out[-1]


def _blocks_wait(sems, lands, after, name):
    n = len(lands)
    n_sem = 3 * n

    def body(*refs):
        lnd = refs[:n]
        s_sems, r_sems = refs[n:n + n_sem], refs[n + n_sem:n + 2 * n_sem]
        x, y, c, chips = _place()
        me_chip = 2 * x + y
        for k in range(n):
            for j, (px, py) in enumerate(chips):
                cp = pltpu.make_async_remote_copy(
                    src_ref=_half_rows(lnd[k], me_chip, c), dst_ref=_half_rows(lnd[k], 2 * px + py, c),
                    send_sem=s_sems[3 * k + j], recv_sem=r_sems[3 * k + j],
                    device_id=(px, py, c), device_id_type=MESH)
                cp.wait_send()
                cp.wait_recv()

    out = pl.pallas_call(
        body, name=name,
        out_shape=tuple(pltpu.HBM(l.shape, l.dtype) for l in lands),
        in_specs=[HBM_SPEC] * n + [SEM_SPEC] * (2 * n_sem) + [ANY_SPEC] * len(after),
        out_specs=[HBM_SPEC] * n,
        input_output_aliases={i: i for i in range(n)},
        compiler_params=pltpu.CompilerParams(has_side_effects=EFFECT),
    )(*lands, *sems, *after)
    return list(out)


def _forward_start(lands, name):
    n = len(lands)
    n_sem = 3 * n

    def body(*refs):
        lnd = refs[:n]
        send_sems, recv_sems = refs[n:n + n_sem], refs[n + n_sem:n + 2 * n_sem]
        x, y, c, chips = _place()
        for k in range(n):
            for j, (px, py) in enumerate(chips):
                pltpu.make_async_remote_copy(
                    src_ref=_half_rows(lnd[k], 2 * px + py, c), dst_ref=_half_rows(lnd[k], 2 * px + py, c),
                    send_sem=send_sems[3 * k + j], recv_sem=recv_sems[3 * k + j],
                    device_id=(x, y, 1 - c), device_id_type=MESH).start()
        refs[-1][...] = jnp.zeros_like(refs[-1])

    out = pl.pallas_call(
        body, name=name,
        out_shape=(*[pltpu.SemaphoreType.DMA(())] * (2 * n_sem), *[pltpu.HBM(l.shape, l.dtype) for l in lands],
                   jax.ShapeDtypeStruct((8, 128), F32)),
        in_specs=[HBM_SPEC] * n,
        out_specs=(*[SEM_SPEC] * (2 * n_sem), *[HBM_SPEC] * n, VMEM_SPEC),
        input_output_aliases={i: 2 * n_sem + i for i in range(n)},
        compiler_params=pltpu.CompilerParams(has_side_effects=EFFECT),
    )(*[_hbm(l) for l in lands])
    return list(out[:2 * n_sem]), list(out[2 * n_sem:2 * n_sem + n]), out[-1]


def _forward_wait(sems, lands, after, name):
    n = len(lands)
    n_sem = 3 * n

    def body(*refs):
        lnd = refs[:n]
        s_sems, r_sems = refs[n:n + n_sem], refs[n + n_sem:n + 2 * n_sem]
        x, y, c, chips = _place()
        for k in range(n):
            for j, (px, py) in enumerate(chips):
                cp = pltpu.make_async_remote_copy(
                    src_ref=_half_rows(lnd[k], 2 * px + py, c), dst_ref=_half_rows(lnd[k], 2 * px + py, 1 - c),
                    send_sem=s_sems[3 * k + j], recv_sem=r_sems[3 * k + j],
                    device_id=(x, y, 1 - c), device_id_type=MESH)
                cp.wait_send()
                cp.wait_recv()

    out = pl.pallas_call(
        body, name=name,
        out_shape=tuple(pltpu.HBM(l.shape, l.dtype) for l in lands),
        in_specs=[HBM_SPEC] * n + [SEM_SPEC] * (2 * n_sem) + [ANY_SPEC] * len(after),
        out_specs=[HBM_SPEC] * n,
        input_output_aliases={i: i for i in range(n)},
        compiler_params=pltpu.CompilerParams(has_side_effects=EFFECT),
    )(*lands, *sems, *after)
    return list(out)


def _blocks_finish(lands, name):
    n = len(lands)

    def body(*refs):
        lnd = refs[n:2 * n]
        send_sems, recv_sems = refs[2 * n:]
        x, y, c, chips = _place()
        sibling = (x, y, 1 - c)

        def copy(k, j, chip_id, pc):
            return pltpu.make_async_remote_copy(
                src_ref=_half_rows(lnd[k], chip_id, pc), dst_ref=_half_rows(lnd[k], chip_id, pc),
                send_sem=send_sems.at[k, j], recv_sem=recv_sems.at[k, j], device_id=sibling, device_id_type=MESH)

        started = []
        for k in range(n):
            for j, (px, py) in enumerate(chips):
                cp = copy(k, j, 2 * px + py, c)
                cp.start()
                started.append(cp)
        for k in range(n):
            for j, (px, py) in enumerate(chips):
                copy(k, j, 2 * px + py, 1 - c).wait_recv()
        for cp in started:
            cp.wait_send()

    out = pl.pallas_call(
        body, name=name,
        out_shape=[jax.ShapeDtypeStruct(l.shape, l.dtype) for l in lands],
        in_specs=[ANY_SPEC] * n, out_specs=[ANY_SPEC] * n,
        input_output_aliases={i: i for i in range(n)},
        scratch_shapes=[pltpu.SemaphoreType.DMA((n, 3)), pltpu.SemaphoreType.DMA((n, 3))],
    )(*lands)
    return list(out)


def _send_half_start(arrs, name, after=()):
    n = len(arrs)
    first = 2 * n + len(after)

    def body(*refs):
        ins, lnd = refs[:n], refs[n:2 * n]
        send_sems, recv_sems = refs[first:first + n], refs[first + n:first + 2 * n]
        token = refs[-1]
        x, y, c, _ = _place()
        for k in range(n):
            pltpu.make_async_remote_copy(
                src_ref=_half_of(ins[k], 1 - c), dst_ref=lnd[k], send_sem=send_sems[k], recv_sem=recv_sems[k],
                device_id=(x, y, 1 - c), device_id_type=MESH).start()
        token[...] = jnp.zeros_like(token)

    lands = [_hbm(lax.empty(_half_shape(a.shape), a.dtype)) for a in arrs]
    out = pl.pallas_call(
        body, name=name,
        out_shape=(*[pltpu.SemaphoreType.DMA(())] * (2 * n), *[pltpu.HBM(a.shape, a.dtype) for a in arrs],
                   *[pltpu.HBM(l.shape, l.dtype) for l in lands], jax.ShapeDtypeStruct((8, 128), F32)),
        in_specs=[HBM_SPEC] * (2 * n) + [ANY_SPEC] * len(after),
        out_specs=(*[SEM_SPEC] * (2 * n), *[HBM_SPEC] * (2 * n), VMEM_SPEC),
        input_output_aliases={i: 2 * n + i for i in range(2 * n)},
        compiler_params=pltpu.CompilerParams(has_side_effects=EFFECT),
    )(*[_hbm(a) for a in arrs], *lands, *after)
    return list(out[:2 * n]), list(out[2 * n:3 * n]), list(out[3 * n:4 * n]), out[-1]


def _send_half_wait(sems, arrs, lands, after, name):
    n = len(arrs)

    def body(*refs):
        ins, lnd = refs[:n], refs[n:2 * n]
        s_sems, r_sems = refs[2 * n:3 * n], refs[3 * n:4 * n]
        x, y, c, _ = _place()
        for k in range(n):
            cp = pltpu.make_async_remote_copy(
                src_ref=_half_of(ins[k], 1 - c), dst_ref=lnd[k], send_sem=s_sems[k], recv_sem=r_sems[k],
                device_id=(x, y, 1 - c), device_id_type=MESH)
            cp.wait_send()
            cp.wait_recv()

    out = pl.pallas_call(
        body, name=name,
        out_shape=tuple(pltpu.HBM(a.shape, a.dtype) for a in list(arrs) + list(lands)),
        in_specs=[HBM_SPEC] * (2 * n) + [SEM_SPEC] * (2 * n) + [ANY_SPEC] * len(after),
        out_specs=[HBM_SPEC] * (2 * n),
        input_output_aliases={i: i for i in range(2 * n)},
        compiler_params=pltpu.CompilerParams(has_side_effects=EFFECT),
    )(*arrs, *lands, *sems, *after)
    return list(out[:n]), list(out[n:])


def _scatter_start(arrs, name, after=()):
    n = len(arrs)
    n_sem = 3 * n
    first = 2 * n + len(after)

    def body(*refs):
        ins, lnd = refs[:n], refs[n:2 * n]
        send_sems, recv_sems = refs[first:first + n_sem], refs[first + n_sem:first + 2 * n_sem]
        token = refs[-1]
        x, y, c, chips = _place()
        me_chip = 2 * x + y
        for k in range(n):
            for j, (px, py) in enumerate(chips):
                pltpu.make_async_remote_copy(
                    src_ref=ins[k].at[2 * px + py], dst_ref=lnd[k].at[me_chip],
                    send_sem=send_sems[3 * k + j], recv_sem=recv_sems[3 * k + j],
                    device_id=(px, py, c), device_id_type=MESH).start()
        token[...] = jnp.zeros_like(token)

    lands = [_hbm(lax.empty(a.shape, a.dtype)) for a in arrs]
    out = pl.pallas_call(
        body, name=name,
        out_shape=(*[pltpu.SemaphoreType.DMA(())] * (2 * n_sem),
                   *[pltpu.HBM(a.shape, a.dtype) for a in arrs], *[pltpu.HBM(a.shape, a.dtype) for a in arrs],
                   jax.ShapeDtypeStruct((8, 128), F32)),
        in_specs=[HBM_SPEC] * (2 * n) + [ANY_SPEC] * len(after),
        out_specs=(*[SEM_SPEC] * (2 * n_sem), *[HBM_SPEC] * (2 * n), VMEM_SPEC),
        input_output_aliases={i: 2 * n_sem + i for i in range(2 * n)},
        compiler_params=pltpu.CompilerParams(has_side_effects=EFFECT),
    )(*[_hbm(a) for a in arrs], *lands, *after)
    base = 2 * n_sem
    return list(out[:base]), list(out[base:base + n]), list(out[base + n:base + 2 * n]), out[-1]


def _scatter_wait(sems, arrs, lands, after, name):
    n = len(arrs)
    n_sem = 3 * n

    def body(*refs):
        ins, lnd = refs[:n], refs[n:2 * n]
        s_sems, r_sems = refs[2 * n:2 * n + n_sem], refs[2 * n + n_sem:2 * n + 2 * n_sem]
        x, y, c, chips = _place()
        for k in range(n):
            for j, (px, py) in enumerate(chips):
                cp = pltpu.make_async_remote_copy(
                    src_ref=ins[k].at[2 * px + py], dst_ref=lnd[k].at[2 * px + py],
                    send_sem=s_sems[3 * k + j], recv_sem=r_sems[3 * k + j],
                    device_id=(px, py, c), device_id_type=MESH)
                cp.wait_send()
                cp.wait_recv()

    out = pl.pallas_call(
        body, name=name,
        out_shape=tuple(pltpu.HBM(a.shape, a.dtype) for a in list(arrs) + list(lands)),
        in_specs=[HBM_SPEC] * (2 * n) + [SEM_SPEC] * (2 * n_sem) + [ANY_SPEC] * len(after),
        out_specs=[HBM_SPEC] * (2 * n),
        input_output_aliases={i: i for i in range(2 * n)},
        compiler_params=pltpu.CompilerParams(has_side_effects=EFFECT),
    )(*arrs, *lands, *sems, *after)
    return list(out[:n]), list(out[n:])


def _sum_owner(chip_arr, pairs, got, name):
    nb, h, cols = got.shape
    tr = _row_tile(h, 16, 256)

    def body(chip_ref, own_ref, a_ref, b_ref, c_ref, o_ref):
        o_ref[...] = ((own_ref[0].astype(F32) + a_ref[0].astype(F32)) + b_ref[0].astype(F32)) + c_ref[0].astype(F32)

    def slot(off):
        return pl.BlockSpec((1, tr, cols), lambda i, chip_ref: ((chip_ref[0] + off) % N_CHIP, i, 0))

    return pl.pallas_call(
        body, name=name,
        grid_spec=pltpu.PrefetchScalarGridSpec(
            num_scalar_prefetch=1, grid=(h // tr,),
            in_specs=[slot(0), slot(1), slot(2), slot(3)],
            out_specs=pl.BlockSpec((tr, cols), lambda i, chip_ref: (i, 0))),
        out_shape=jax.ShapeDtypeStruct((h, cols), F32),
        compiler_params=_cparams(dimension_semantics=("parallel",)),
    )(chip_arr, pairs, got, got, got)


def _swap_start(arrs, name, after=()):
    n = len(arrs)
    first = 2 * n + len(after)

    def body(*refs):
        ins, lnd = refs[:n], refs[n:2 * n]
        send_sems, recv_sems = refs[first:first + n], refs[first + n:first + 2 * n]
        x, y, c, _ = _place()
        for k in range(n):
            pltpu.make_async_remote_copy(
                src_ref=ins[k], dst_ref=lnd[k], send_sem=send_sems[k], recv_sem=recv_sems[k],
                device_id=(x, y, 1 - c), device_id_type=MESH).start()

    lands = [_hbm(lax.empty(a.shape, a.dtype)) for a in arrs]
    out = pl.pallas_call(
        body, name=name,
        out_shape=(*[pltpu.SemaphoreType.DMA(())] * (2 * n), *[pltpu.HBM(a.shape, a.dtype) for a in arrs],
                   *[pltpu.HBM(a.shape, a.dtype) for a in arrs]),
        in_specs=[HBM_SPEC] * (2 * n) + [ANY_SPEC] * len(after),
        out_specs=(*[SEM_SPEC] * (2 * n), *[HBM_SPEC] * (2 * n)),
        input_output_aliases={i: 2 * n + i for i in range(2 * n)},
        compiler_params=pltpu.CompilerParams(has_side_effects=EFFECT),
    )(*[_hbm(a) for a in arrs], *lands, *after)
    return list(out[:2 * n]), list(out[2 * n:3 * n]), list(out[3 * n:4 * n])


def _swap_wait(sems, arrs, lands, after, name):
    n = len(arrs)

    def body(*refs):
        ins, lnd = refs[:n], refs[n:2 * n]
        s_sems, r_sems = refs[2 * n:3 * n], refs[3 * n:4 * n]
        x, y, c, _ = _place()
        for k in range(n):
            cp = pltpu.make_async_remote_copy(
                src_ref=ins[k], dst_ref=lnd[k], send_sem=s_sems[k], recv_sem=r_sems[k],
                device_id=(x, y, 1 - c), device_id_type=MESH)
            cp.wait_send()
            cp.wait_recv()

    out = pl.pallas_call(
        body, name=name,
        out_shape=tuple(pltpu.HBM(a.shape, a.dtype) for a in list(arrs) + list(lands)),
        in_specs=[HBM_SPEC] * (2 * n) + [SEM_SPEC] * (2 * n) + [ANY_SPEC] * len(after),
        out_specs=[HBM_SPEC] * (2 * n),
        input_output_aliases={i: i for i in range(2 * n)},
        compiler_params=pltpu.CompilerParams(has_side_effects=EFFECT),
    )(*arrs, *lands, *sems, *after)
    return list(out[:n]), list(out[n:])


def _row_tile(h, mult=8, cap=512):
    for t in range(cap - cap % mult, mult - 1, -mult):
        if h % t == 0:
            return t
    if mult > 8:
        return _row_tile(h, 8, cap)
    raise ValueError(h)


def _pair_sum(c_arr, full, recv, name):
    nb, rows, cols = full.shape

    def body(c_ref, f_ref, r_ref, o_ref):
        o_ref[...] = (f_ref[...] + r_ref[...]).astype(BF16)

    if _halved_by_rows(full.shape):
        h = rows // 2
        tr = _row_tile(h, 16, 512)
        steps = h // tr
        own = pl.BlockSpec((1, tr, cols), lambda b, i, c_ref: (b, c_ref[0] * steps + i, 0))
        half = pl.BlockSpec((1, tr, cols), lambda b, i, c_ref: (b, i, 0))
    else:
        steps = 1
        own = pl.BlockSpec((1, rows, cols // 2), lambda b, i, c_ref: (b, 0, c_ref[0]))
        half = pl.BlockSpec((1, rows, cols // 2), lambda b, i, c_ref: (b, 0, 0))
    return pl.pallas_call(
        body, name=name,
        grid_spec=pltpu.PrefetchScalarGridSpec(
            num_scalar_prefetch=1, grid=(nb, steps), in_specs=[own, half], out_specs=half),
        out_shape=jax.ShapeDtypeStruct(_half_shape(full.shape), BF16),
        compiler_params=_cparams(dimension_semantics=("parallel", "parallel")),
    )(c_arr, full, recv)


def _adam_math(g, w, m, v):
    m1 = ADAM_B1 * m + (1.0 - ADAM_B1) * g
    v1 = ADAM_B2 * v + (1.0 - ADAM_B2) * (g * g)
    m_hat = m1 / (1.0 - ADAM_B1 ** ADAM_STEP)
    v_hat = v1 / (1.0 - ADAM_B2 ** ADAM_STEP)
    delta = -ADAM_LR * (m_hat / (jnp.sqrt(v_hat) + ADAM_EPS) + ADAM_WD * w)
    return delta, m1, v1


def _adamw_halves(c_arr, own, other, w, m, v, name):
    rows, cols = w.shape
    by_rows = own.shape[1] == cols

    def body(c_ref, own_ref, oth_ref, w_ref, m_ref, v_ref, g_out, d_out, m_out, v_out):
        if by_rows:
            g = jnp.where(pl.program_id(0) == c_ref[0], own_ref[...], oth_ref[...])
        else:
            own_, oth_ = own_ref[...], oth_ref[...]
            g = jnp.where(c_ref[0] == 0, jnp.concatenate([own_, oth_], axis=1), jnp.concatenate([oth_, own_], axis=1))
        d, m1, v1 = _adam_math(g, w_ref[...], m_ref[...], v_ref[...])
        g_out[...] = g
        d_out[...] = d
        m_out[...] = m1
        v_out[...] = v1

    if by_rows:
        h = rows // 2
        tr = _row_tile(h)
        steps = h // tr
        grid = (2, steps)
        half_spec = pl.BlockSpec((tr, cols), lambda p, i, c_ref: (i, 0))
        full_spec = pl.BlockSpec((tr, cols), lambda p, i, c_ref: (p * steps + i, 0))
    else:
        tr = _row_tile(rows)
        grid = (1, rows // tr)
        half_spec = pl.BlockSpec((tr, cols // 2), lambda p, i, c_ref: (i, 0))
        full_spec = pl.BlockSpec((tr, cols), lambda p, i, c_ref: (i, 0))
    return pl.pallas_call(
        body, name=name,
        grid_spec=pltpu.PrefetchScalarGridSpec(
            num_scalar_prefetch=1, grid=grid,
            in_specs=[half_spec, half_spec, full_spec, full_spec, full_spec],
            out_specs=[full_spec] * 4),
        out_shape=[jax.ShapeDtypeStruct(w.shape, F32)] * 4,
        compiler_params=_cparams(dimension_semantics=("parallel", "parallel")),
    )(c_arr, own, other, w, m, v)


def _adamw_whole(items, name):
    n = len(items)

    def body(*refs):
        ins, outs = refs[:4 * n], refs[4 * n:]
        for k in range(n):
            g, w, m, v = (r[...] for r in ins[4 * k:4 * k + 4])
            d, m1, v1 = _adam_math(g, w, m, v)
            outs[3 * k][...] = d
            outs[3 * k + 1][...] = m1
            outs[3 * k + 2][...] = v1

    flat = [a for it in items for a in it]
    shapes = [jax.ShapeDtypeStruct(it[1].shape, F32) for it in items for _ in range(3)]
    out = pl.pallas_call(
        body, name=name, out_shape=shapes,
        in_specs=[VMEM_SPEC] * (4 * n), out_specs=[VMEM_SPEC] * (3 * n),
        compiler_params=_cparams(),
    )(*flat)
    return [tuple(out[3 * k:3 * k + 3]) for k in range(n)]


def _adamw_tiled(g, w, m, v, name):
    rows, cols = w.shape
    tr = _row_tile(rows, cap=256)

    def body(g_ref, w_ref, m_ref, v_ref, d_out, m_out, v_out):
        d, m1, v1 = _adam_math(g_ref[...], w_ref[...], m_ref[...], v_ref[...])
        d_out[...] = d
        m_out[...] = m1
        v_out[...] = v1

    spec = pl.BlockSpec((tr, cols), lambda i: (i, 0))
    return pl.pallas_call(
        body, name=name, grid=(rows // tr,),
        out_shape=[jax.ShapeDtypeStruct(w.shape, F32)] * 3,
        in_specs=[spec] * 4, out_specs=[spec] * 3,
        compiler_params=_cparams(dimension_semantics=("parallel",)),
    )(g, w, m, v)


def _mod_forward(cond, w_mod, b_mod_cols, name):
    def body(c_ref, w_ref, b_ref, o_ref):
        o_ref[...] = _dot(_silu(c_ref[...]), w_ref[...]) + b_ref[...]

    return pl.pallas_call(
        body, name=name, out_shape=jax.ShapeDtypeStruct((cond.shape[0], w_mod.shape[1]), F32),
        in_specs=[VMEM_SPEC] * 3, out_specs=VMEM_SPEC, compiler_params=_cparams(),
    )(cond, w_mod, b_mod_cols)


def _mod_backward(cond, w_mod, dmod_cols, name):
    def body(c_ref, w_ref, d_ref, gw_ref, gc_ref):
        s = _silu(c_ref[...])
        d = d_ref[...]
        gw_ref[...] = _dot_tn(s, d)
        gc_ref[...] = _dot_nt(d[8:16, :], w_ref[...])

    return pl.pallas_call(
        body, name=name,
        out_shape=[jax.ShapeDtypeStruct(w_mod.shape, F32), jax.ShapeDtypeStruct((8, w_mod.shape[0]), F32)],
        in_specs=[VMEM_SPEC] * 3, out_specs=[VMEM_SPEC] * 2, compiler_params=_cparams(),
    )(cond, w_mod, dmod_cols)


def _col_chunks(width, step=512):
    return [(s, min(step, width - s)) for s in range(0, width, step)]


def _w_in_row(p_off):
    if p_off < 9 * HW:
        return p_off
    return 9 * HW if p_off == OFF_LR else p_off + 2 * RANK


def _in_projection(ctx0, x0, modc, modx, pre1, w_t, n_ctx_tiles, name):
    d = x0.shape[1]
    rows = ctx0.shape[0] + x0.shape[0]
    width = P_WIDTH

    def body(ctx_ref, x_ref, modc_ref, modx_ref, pre_ref, w_ref, h_ref, p_ref):
        is_ctx = pl.program_id(0) < n_ctx_tiles
        n, _ = _rms(jnp.where(is_ctx, ctx_ref[...], x_ref[...]))
        shift = jnp.where(is_ctx, modc_ref[0:1, :], modx_ref[0:1, :])
        scale = jnp.where(is_ctx, modc_ref[1:2, :], modx_ref[1:2, :])
        h = (n * pre_ref[...] * (1.0 + scale) + shift).astype(BF16)
        h_ref[...] = h
        for s, w in _col_chunks(width):
            p_ref[:, s:s + w] = _dot_nt(h, w_ref[_w_in_row(s):_w_in_row(s) + w, :])

    row = lambda i: (i, 0)
    fixed = lambda i: (0, 0)
    return pl.pallas_call(
        body, name=name, grid=(rows // TM,),
        out_shape=[jax.ShapeDtypeStruct((rows, d), BF16), jax.ShapeDtypeStruct((rows, width), F32)],
        in_specs=[pl.BlockSpec((TM, d), lambda i: (jnp.minimum(i, n_ctx_tiles - 1), 0)),
                  pl.BlockSpec((TM, d), lambda i: (jnp.maximum(i - n_ctx_tiles, 0), 0)),
                  pl.BlockSpec((8, d), fixed), pl.BlockSpec((8, d), fixed), pl.BlockSpec((1, d), fixed), VMEM_SPEC],
        out_specs=[pl.BlockSpec((TM, d), row), pl.BlockSpec((TM, width), row)],
        compiler_params=_cparams(dimension_semantics=("parallel",)),
    )(ctx0, x0, modc, modx, pre1, w_t)


C_HQ, C_HI, C_HF_FW, C_HF_BW, C_HGATE, C_GQ, C_GK, C_GV, C_GGATE = range(9)
OFF_GATE_HG = 9 * HW
OFF_LR = 13 * HW
P_WIDTH = OFF_LR + 128


def _head_norm_fwd(o, w):
    outs, ns, rs = [], [], []
    for h in range(NH):
        n, r = _rms(o[:, h * HD:(h + 1) * HD])
        ns.append(n)
        rs.append(r)
        outs.append(n * w)
    return jnp.concatenate(outs, axis=1), ns, rs


def _mixer_tail(z, o_hg, o_gla, p_hgate, p_ggate, p_gate_hg, p_gate_gla, hg_on, gla_on, wbh, wbg, wout):
    on_hg, n_hg, r_hg = _head_norm_fwd(o_hg, hg_on)
    on_gla, n_gla, r_gla = _head_norm_fwd(o_gla, gla_on)
    og_hg = (on_hg * _silu(p_hgate)).astype(BF16)
    og_gla = (on_gla * _silu(p_ggate)).astype(BF16)
    b_hg = jnp.dot(og_hg, wbh, preferred_element_type=F32)
    b_gla = jnp.dot(og_gla, wbg, preferred_element_type=F32)
    s_hg = _sigmoid(p_gate_hg)
    s_gla = _sigmoid(p_gate_gla)
    merged = (s_hg * b_hg + s_gla * b_gla).astype(BF16)
    y1 = jnp.dot(merged, wout, preferred_element_type=F32)
    return dict(on_hg=on_hg, n_hg=n_hg, r_hg=r_hg, on_gla=on_gla, n_gla=n_gla, r_gla=r_gla, og_hg=og_hg,
                og_gla=og_gla, b_hg=b_hg, b_gla=b_gla, s_hg=s_hg, s_gla=s_gla, merged=merged, y1=y1)


def _mixer_tail_fwd(x_lat, p, o_list, modx, norms, onorms, w_br_hg, w_br_gla, w_out, n_ctx_tiles, name):
    rows, d = x_lat.shape

    def body(x_ref, ofw_hg, obw_hg, ofw_gla, obw_gla, p_hgate, p_ggate, p_ghg_a, p_ghg_b, p_ggla_a, p_ggla_b,
             modx_ref, norm_ref, on_ref, wbh_ref, wbg_ref, wout_ref, z2_ref, y1_ref, mrg_ref, oghg_ref, oggla_ref):
        p_gate_hg = jnp.concatenate([p_ghg_a[...], p_ghg_b[...]], axis=1)
        p_gate_gla = jnp.concatenate([p_ggla_a[...], p_ggla_b[...]], axis=1)
        t = _mixer_tail(x_ref[...], ofw_hg[...] + obw_hg[...], ofw_gla[...] + obw_gla[...], p_hgate[...],
                        p_ggate[...], p_gate_hg, p_gate_gla, on_ref[0:1, 0:HD], on_ref[1:2, 0:HD],
                        wbh_ref[...], wbg_ref[...], wout_ref[...])
        y1_ref[...] = t["y1"]
        mrg_ref[...] = t["merged"]
        oghg_ref[...] = t["og_hg"]
        oggla_ref[...] = t["og_gla"]
        n1, _ = _rms(t["y1"])
        z2_ref[...] = x_ref[...] + n1 * norm_ref[1:2, :] * modx_ref[2:3, :]

    lat = lambda i: (i, 0)
    full = lambda i: (i + n_ctx_tiles, 0)
    fixed = lambda i: (0, 0)

    def pcol(blk):
        return pl.BlockSpec((TM, HW), lambda i: (i + n_ctx_tiles, blk))

    in_specs = ([pl.BlockSpec((TM, d), lat)] + [pl.BlockSpec((TM, HW), full)] * 4
                + [pcol(C_HGATE), pcol(C_GGATE), pcol(9), pcol(10), pcol(11), pcol(12)]
                + [pl.BlockSpec((8, d), fixed)] * 3 + [VMEM_SPEC] * 3)
    bf = lambda w: jax.ShapeDtypeStruct((rows, w), BF16)
    f32 = jax.ShapeDtypeStruct((rows, d), F32)
    return pl.pallas_call(
        body, name=name, grid=(rows // TM,), out_shape=[f32, f32, bf(d), bf(HW), bf(HW)], in_specs=in_specs,
        out_specs=[pl.BlockSpec((TM, d), lat)] * 3 + [pl.BlockSpec((TM, HW), lat)] * 2,
        compiler_params=_cparams(dimension_semantics=("parallel",)),
    )(x_lat, *o_list, p, p, p, p, p, p, modx, norms, onorms, w_br_hg, w_br_gla, w_out)


def _ffn_fwd_bwd(z2, modx, norms, w_gate, w_up, w_down, target, name):
    rows, d = z2.shape
    dff = w_gate.shape[0]
    inv_d = 1.0 / d

    def body(z2_ref, modx_ref, norm_ref, wg_ref, wu_ref, wd_ref, t_ref,
             loss_ref, dz2_ref, h2_ref, a_ref, du_ref, dv_ref, dy2_ref, stat_ref):
        i = pl.program_id(0)
        pre2, post2 = norm_ref[2:3, :], norm_ref[3:4, :]
        shift2, scale2, gate2 = modx_ref[3:4, :], modx_ref[4:5, :], modx_ref[5:6, :]
        z2 = z2_ref[...]
        n2, r2 = _rms(z2)
        nw2 = n2 * pre2
        h2 = (nw2 * (1.0 + scale2) + shift2).astype(BF16)
        h2_ref[...] = h2
        u = _dot_nt(h2, wg_ref[...])
        v = _dot_nt(h2, wu_ref[...])
        su = _silu(u)
        a = (su * v).astype(BF16)
        a_ref[...] = a
        y2 = jnp.dot(a, wd_ref[...], preferred_element_type=F32)
        n3, r3 = _rms(y2)
        err = z2 + n3 * post2 * gate2 - t_ref[...]
        part = 0.5 * inv_d * jnp.sum(err * err)
        dz3 = err * inv_d
        dgate2 = _colsum(dz3 * n3 * post2)
        tt = dz3 * gate2
        dpost2 = _colsum(tt * n3)
        dy2 = _rms_bwd(tt * post2, n3, r3).astype(BF16)
        dy2_ref[...] = dy2
        da = _dot_nt(dy2, wd_ref[...])
        du = (da * v * _dsilu(u)).astype(BF16)
        dv = (da * su).astype(BF16)
        du_ref[...] = du
        dv_ref[...] = dv
        dh2 = (jnp.dot(du, wg_ref[...], preferred_element_type=F32)
               + jnp.dot(dv, wu_ref[...], preferred_element_type=F32))
        dshift2 = _colsum(dh2)
        dscale2 = _colsum(dh2 * nw2)
        dnw2 = dh2 * (1.0 + scale2)
        dpre2 = _colsum(dnw2 * n2)
        dz2_ref[...] = dz3 + _rms_bwd(dnw2 * pre2, n2, r2)

        @pl.when(i == 0)
        def _():
            stat_ref[...] = jnp.zeros_like(stat_ref)
            loss_ref[...] = jnp.zeros_like(loss_ref)

        for r, val in enumerate((dshift2, dscale2, dgate2, dpre2, dpost2)):
            stat_ref[r:r + 1, :] += val
        loss_ref[...] += part
        stat_ref[5:6, 0:128] += part

    lat = lambda i: (i, 0)
    fixed = lambda i: (0, 0)
    bf = lambda w: jax.ShapeDtypeStruct((rows, w), BF16)
    return pl.pallas_call(
        body, name=name, grid=(rows // TM,),
        out_shape=[jax.ShapeDtypeStruct((8, 128), F32), jax.ShapeDtypeStruct((rows, d), F32), bf(d), bf(dff), bf(dff),
                   bf(dff), bf(d), jax.ShapeDtypeStruct((8, d), F32)],
        in_specs=[pl.BlockSpec((TM, d), lat), pl.BlockSpec((8, d), fixed), pl.BlockSpec((8, d), fixed)]
        + [VMEM_SPEC] * 3 + [pl.BlockSpec((TM, d), lat)],
        out_specs=[pl.BlockSpec((8, 128), fixed), pl.BlockSpec((TM, d), lat), pl.BlockSpec((TM, d), lat),
                   pl.BlockSpec((TM, dff), lat), pl.BlockSpec((TM, dff), lat), pl.BlockSpec((TM, dff), lat),
                   pl.BlockSpec((TM, d), lat), pl.BlockSpec((8, d), fixed)],
        compiler_params=_cparams(dimension_semantics=("arbitrary",)),
    )(z2, modx, norms, w_gate, w_up, w_down, target)


def _mixer_tail_bwd(x_lat, p, o_list, dz2, y1, modx, norms, onorms, w_br_hg, w_br_gla, w_out, n_ctx_tiles, n_tiles,
                    name):
    rows, d = x_lat.shape
    total = n_tiles * TM

    def body(x_ref, ofw_hg, obw_hg, ofw_gla, obw_gla, p_hgate, p_ggate, p_ghg_a, p_ghg_b, p_ggla_a, p_ggla_b,
             dz2_ref, y1_ref, modx_ref, norm_ref, on_ref, wbh_ref, wbg_ref, wout_ref,
             dohg_ref, dogla_ref, dhgate_ref, dggate_ref, dghg_ref, dggla_ref, dy1_ref, dbhg_ref, dbgla_ref,
             stat_ref):
        i = pl.program_id(0)

        @pl.when(i == 0)
        def _():
            stat_ref[...] = jnp.zeros_like(stat_ref)

        @pl.when(i < n_ctx_tiles)
        def _():
            for ref in (dohg_ref, dogla_ref, dhgate_ref, dggate_ref, dghg_ref, dggla_ref):
                ref[...] = jnp.zeros_like(ref)

        @pl.when(i >= n_ctx_tiles)
        def _():
            post1, gate1 = norm_ref[1:2, :], modx_ref[2:3, :]
            hg_on, gla_on = on_ref[0:1, 0:HD], on_ref[1:2, 0:HD]
            p_gate_hg = jnp.concatenate([p_ghg_a[...], p_ghg_b[...]], axis=1)
            p_gate_gla = jnp.concatenate([p_ggla_a[...], p_ggla_b[...]], axis=1)
            ph, pg = p_hgate[...], p_ggate[...]
            t = _mixer_tail(x_ref[...], ofw_hg[...] + obw_hg[...], ofw_gla[...] + obw_gla[...], ph, pg,
                            p_gate_hg, p_gate_gla, hg_on, gla_on, wbh_ref[...], wbg_ref[...], wout_ref[...])
            dz2 = dz2_ref[...]
            n1, r1 = _rms(y1_ref[...])
            dgate1 = _colsum(dz2 * n1 * post1)
            tt = dz2 * gate1
            dpost1 = _colsum(tt * n1)
            dy1 = _rms_bwd(tt * post1, n1, r1).astype(BF16)
            dy1_ref[...] = dy1
            dmerged = _dot_nt(dy1, wout_ref[...])
            dghg_ref[...] = (dmerged * t["b_hg"] * t["s_hg"] * (1.0 - t["s_hg"])).astype(BF16)
            dggla_ref[...] = (dmerged * t["b_gla"] * t["s_gla"] * (1.0 - t["s_gla"])).astype(BF16)
            db_hg = (dmerged * t["s_hg"]).astype(BF16)
            db_gla = (dmerged * t["s_gla"]).astype(BF16)
            dbhg_ref[...] = db_hg
            dbgla_ref[...] = db_gla
            don_acc = []
            for (db, wb, pgate, on, ns, rs, gain, gate_ref, do_ref) in (
                    (db_hg, wbh_ref, ph, t["on_hg"], t["n_hg"], t["r_hg"], hg_on, dhgate_ref, dohg_ref),
                    (db_gla, wbg_ref, pg, t["on_gla"], t["n_gla"], t["r_gla"], gla_on, dggate_ref, dogla_ref)):
                dog = _dot_nt(db, wb[...])
                gate_ref[...] = (dog * on * _dsilu(pgate)).astype(BF16)
                don = dog * _silu(pgate)
                acc = jnp.zeros((1, HD), F32)
                for h in range(NH):
                    sl = slice(h * HD, (h + 1) * HD)
                    acc = acc + _colsum(don[:, sl] * ns[h])
                    do_ref[:, sl] = _rms_bwd(don[:, sl] * gain, ns[h], rs[h]).astype(BF16)
                don_acc.append(acc)
            stat_ref[0:1, :] += dgate1
            stat_ref[1:2, :] += dpost1
            stat_ref[2:3, 0:HD] += don_acc[0]
            stat_ref[2:3, HD:2 * HD] += don_acc[1]

    lat = lambda i: (jnp.maximum(i - n_ctx_tiles, 0), 0)
    full = lambda i: (i, 0)
    fixed = lambda i: (0, 0)

    def pcol(blk):
        return pl.BlockSpec((TM, HW), lambda i: (i, blk))

    in_specs = ([pl.BlockSpec((TM, d), lat)] + [pl.BlockSpec((TM, HW), full)] * 4
                + [pcol(C_HGATE), pcol(C_GGATE), pcol(9), pcol(10), pcol(11), pcol(12)]
                + [pl.BlockSpec((TM, d), lat), pl.BlockSpec((TM, d), lat)]
                + [pl.BlockSpec((8, d), fixed)] * 3 + [VMEM_SPEC] * 3)
    f = lambda w: jax.ShapeDtypeStruct((total, w), BF16)
    out_shape = [f(HW), f(HW), f(HW), f(HW), f(d), f(d), jax.ShapeDtypeStruct((rows, d), BF16),
                 jax.ShapeDtypeStruct((rows, d), BF16), jax.ShapeDtypeStruct((rows, d), BF16),
                 jax.ShapeDtypeStruct((8, d), F32)]
    out_specs = ([pl.BlockSpec((TM, HW), full)] * 4 + [pl.BlockSpec((TM, d), full)] * 2
                 + [pl.BlockSpec((TM, d), lat)] * 3 + [pl.BlockSpec((8, d), fixed)])
    return pl.pallas_call(
        body, name=name, grid=(n_tiles,), out_shape=out_shape, in_specs=in_specs, out_specs=out_specs,
        compiler_params=_cparams(dimension_semantics=("arbitrary",)),
    )(x_lat, *o_list, p, p, p, p, p, p, dz2, y1, modx, norms, onorms, w_br_hg, w_br_gla, w_out)


def _in_projection_bwd(ctx0, x0, dz2, modc, modx, pre1, w_t, pieces, n_ctx_tiles, name):
    d = x0.shape[1]
    rows = ctx0.shape[0] + x0.shape[0]
    lat_rows = dz2.shape[0]
    width = P_WIDTH
    n_pieces = len(pieces)

    def body(*refs):
        ctx_ref, x_ref, dz2_ref, modc_ref, modx_ref, pre_ref, w_ref = refs[:7]
        (dhq_f, dhq_b, dhi_f, dhi_b, dhf_f, dhf_b, dhgate, dgq_f, dgq_b, dgk_f, dgk_b, dgv_f, dgv_b, dggate,
         dghg, dggla, dlr_f, dlr_b) = refs[7:7 + n_pieces]
        dp_ref, gx_ref, stat_ref = refs[7 + n_pieces:]
        i = pl.program_id(0)
        is_ctx = i < n_ctx_tiles
        z = jnp.where(is_ctx, ctx_ref[...], x_ref[...])
        sections = [
            (0, dhq_f[...] + dhq_b[...]), (HW, dhi_f[...] + dhi_b[...]), (2 * HW, dhf_f[...]), (3 * HW, dhf_b[...]),
            (4 * HW, dhgate[...]), (5 * HW, dgq_f[...] + dgq_b[...]), (6 * HW, dgk_f[...] + dgk_b[...]),
            (7 * HW, dgv_f[...] + dgv_b[...]), (8 * HW, dggate[...]),
            (9 * HW, dghg[:, 0:HW]), (10 * HW, dghg[:, HW:2 * HW]),
            (11 * HW, dggla[:, 0:HW]), (12 * HW, dggla[:, HW:2 * HW]), (OFF_LR, dlr_f[...] + dlr_b[...])]
        dh = jnp.zeros((TM, d), F32)
        for off, val in sections:
            w = val.shape[1]
            vb = val.astype(BF16)
            dp_ref[off:off + w, :] = vb.T
            dh = dh + jnp.dot(vb, w_ref[_w_in_row(off):_w_in_row(off) + w, :], preferred_element_type=F32)
        n, r = _rms(z)
        pre = pre_ref[...]
        scale = jnp.where(is_ctx, modc_ref[1:2, :], modx_ref[1:2, :])
        nw = n * pre
        dshift = _colsum(dh)
        dscale = _colsum(dh * nw)
        dnw = dh * (1.0 + scale)
        dpre = _colsum(dnw * n)
        gx_ref[...] = dz2_ref[...] + _rms_bwd(dnw * pre, n, r)
        zero = jnp.zeros((1, d), F32)

        @pl.when(i == 0)
        def _():
            stat_ref[...] = jnp.zeros_like(stat_ref)

        stat_ref[0:1, :] += jnp.where(is_ctx, zero, dshift)
        stat_ref[1:2, :] += jnp.where(is_ctx, zero, dscale)
        stat_ref[2:3, :] += jnp.where(is_ctx, dshift, zero)
        stat_ref[3:4, :] += jnp.where(is_ctx, dscale, zero)
        stat_ref[4:5, :] += dpre

    full = lambda i: (i, 0)
    lat = lambda i: (jnp.maximum(i - n_ctx_tiles, 0), 0)
    fixed = lambda i: (0, 0)
    piece_specs = [pl.BlockSpec((TM, a.shape[1]), full) for a in pieces]
    in_specs = [pl.BlockSpec((TM, d), lambda i: (jnp.minimum(i, n_ctx_tiles - 1), 0)), pl.BlockSpec((TM, d), lat),
                pl.BlockSpec((TM, d), lat), pl.BlockSpec((8, d), fixed),
                pl.BlockSpec((8, d), fixed), pl.BlockSpec((1, d), fixed), VMEM_SPEC] + piece_specs
    return pl.pallas_call(
        body, name=name, grid=(rows // TM,),
        out_shape=[jax.ShapeDtypeStruct((width, rows), BF16), jax.ShapeDtypeStruct((lat_rows, d), F32),
                   jax.ShapeDtypeStruct((8, d), F32)],
        in_specs=in_specs,
        out_specs=[pl.BlockSpec((width, TM), lambda i: (0, i)), pl.BlockSpec((TM, d), lat),
                   pl.BlockSpec((8, d), fixed)],
        compiler_params=_cparams(dimension_semantics=("arbitrary",)),
    )(ctx0, x0, dz2, modc, modx, pre1, w_t, *pieces)


def _transposed_lhs_matmul(x_ref, dy_ref, o_ref, xt_ref):
    @pl.when(pl.program_id(1) == 0)
    def _():
        xt_ref[...] = x_ref[...].T

    o_ref[...] = jnp.dot(xt_ref[...], dy_ref[...], preferred_element_type=F32)


def _w_in_grad(dp_t, h1, n_cols, name, after=()):
    rows, d = h1.shape
    n_main = OFF_LR // HW
    lr0 = _w_in_row(OFF_LR)

    def body(x_ref, xlr_ref, h_ref, *rest):
        o_hbm, acc_ref, sems = rest[len(after):]
        i = pl.program_id(0)
        slot = i % 2

        def main_copy(step):
            row = jnp.where(step < 9, step * HW, step * HW + 2 * RANK)
            return pltpu.make_async_copy(acc_ref.at[step % 2], o_hbm.at[pl.ds(pl.multiple_of(row, 8), HW), :],
                                         sems.at[step % 2])

        @pl.when(i > 1)
        def _():
            main_copy(i - 2).wait()

        @pl.when(i < n_main)
        def _():
            acc_ref[slot] = jnp.dot(x_ref[...], h_ref[...], preferred_element_type=F32)
            main_copy(i).start()

        @pl.when(i == n_main)
        def _():
            acc_ref[slot, 0:128, :] = jnp.dot(xlr_ref[...], h_ref[...], preferred_element_type=F32)
            lr_copy = pltpu.make_async_copy(acc_ref.at[slot, 0:2 * RANK, :], o_hbm.at[lr0:lr0 + 2 * RANK, :],
                                            sems.at[slot])
            lr_copy.start()
            main_copy(i - 1).wait()
            lr_copy.wait()

    return pl.pallas_call(
        body, name=name, grid=(n_main + 1,),
        out_shape=jax.ShapeDtypeStruct((n_cols, d), F32),
        in_specs=[pl.BlockSpec((HW, rows), lambda i: (jnp.minimum(i, n_main - 1), 0)),
                  pl.BlockSpec((128, rows), lambda i: (OFF_LR // 128, 0)),
                  pl.BlockSpec((rows, d), lambda i: (0, 0))] + [ANY_SPEC] * len(after),
        out_specs=ANY_SPEC,
        scratch_shapes=[pltpu.VMEM((2, HW, d), F32), pltpu.SemaphoreType.DMA((2,))],
        compiler_params=_cparams(dimension_semantics=("arbitrary",)),
    )(dp_t, dp_t, h1, *after)


def _weight_grad(xs, dy, name, tk=None, tn=512, k_first=0, k_tiles=None):
    rows = dy.shape[0]
    n = dy.shape[1]
    tn_ = min(tn, n)
    tk_ = xs.shape[1] if tk is None else tk
    k_tiles = xs.shape[1] // tk_ if k_tiles is None else k_tiles
    k = k_tiles * tk_

    return pl.pallas_call(
        functools.partial(_transposed_lhs_matmul), name=name, grid=(k_tiles, n // tn_),
        out_shape=jax.ShapeDtypeStruct((k, n), F32),
        in_specs=[pl.BlockSpec((rows, tk_), lambda i, j: (0, i + k_first)),
                  pl.BlockSpec((rows, tn_), lambda i, j: (0, j))],
        out_specs=pl.BlockSpec((tk_, tn_), lambda i, j: (i, j)),
        scratch_shapes=[pltpu.VMEM((tk_, rows), BF16)],
        compiler_params=_cparams(dimension_semantics=("parallel", "arbitrary")),
    )(xs, dy)


def _running_sums(xs, fws):
    c = xs[0].shape[0]
    row = lax.broadcasted_iota(jnp.int32, (c, 1), 0)
    s = 1
    while s < c:
        xs = [x + (jnp.where(row >= s, pltpu.roll(x, s, axis=0), 0.0) if fw else
                   jnp.where(row < c - s, pltpu.roll(x, c - s, axis=0), 0.0)) for x, fw in zip(xs, fws)]
        s *= 2
    return xs


def _chunks_terms(qs, ks, gs, fws):
    c = CHUNK
    n = len(qs)
    r = lax.broadcasted_iota(jnp.int32, (c, c), 0)
    s = lax.broadcasted_iota(jnp.int32, (c, c), 1)
    row = lax.broadcasted_iota(jnp.int32, (c, 1), 0)
    per_dir = {}
    for fw in set(fws):
        pos = row if fw else (c - 1 - row)
        per_dir[fw] = dict(
            causal=(s <= r) if fw else (s >= r), causal_t=(s >= r) if fw else (s <= r), pos=pos,
            in_blk=[(pos >= SUB * j) & (pos < SUB * (j + 1)) for j in range(NSUB)],
            start_row=[None] + [SUB * j - 1 if fw else c - SUB * j for j in range(1, NSUB)],
            rend=c - 1 if fw else 0)
    dirs = [per_dir[fw] for fw in fws]
    cums = _running_sums(gs, fws)
    starts = [[None] + [cum[d["start_row"][j]:d["start_row"][j] + 1, :] for j in range(1, NSUB)]
              for cum, d in zip(cums, dirs)]
    es = [[jnp.exp(cum) for cum in cums]]
    for j in range(1, NSUB):
        es.append([jnp.exp(jnp.where(d["pos"] >= SUB * j, cum - st[j], -1e30)) for cum, st, d in zip(cums, starts, dirs)])
    owns = [functools.reduce(lambda rest, j: jnp.where(d["in_blk"][j], st[j], rest), range(1, NSUB), 0.0)
            for st, d in zip(starts, dirs)]
    kscales = [jnp.exp(own - cum) for own, cum in zip(owns, cums)]
    cends = [cum[d["rend"]:d["rend"] + 1, :] for cum, d in zip(cums, dirs)]
    tails = [jnp.exp(cend - cum) for cend, cum in zip(cends, cums)]
    qcats = [jnp.concatenate([q * es[j][i] for j in range(NSUB)], axis=1).astype(BF16) for i, q in enumerate(qs)]
    kts = [k * ksc for k, ksc in zip(ks, kscales)]
    kms = [jnp.concatenate([jnp.where(d["in_blk"][j], kt, 0.0) for j in range(NSUB)], axis=1).astype(BF16)
           for kt, d in zip(kts, dirs)]
    e_by_lane = [[es[j][i] for j in range(NSUB)] for i in range(n)]
    return dict(dirs=dirs, e=e_by_lane, kscale=kscales, cend=cends, tail=tails, qcat=qcats, km=kms, kt=kts)


def _chunks_fwd(qs, ks, vs, gs, st0s, fws):
    t = _chunks_terms(qs, ks, gs, fws)
    scores = [_dot_nt(qc, km) for qc, km in zip(t["qcat"], t["km"])]
    a = [jnp.where(d["causal"], sc, 0.0) for sc, d in zip(scores, t["dirs"])]
    inter = [_dot_nt(qc[:, 0:HD], st0) for qc, st0 in zip(t["qcat"], st0s)]
    intra = [_dot(a_, v) for a_, v in zip(a, vs)]
    os_ = [x + y for x, y in zip(intra, inter)]
    upd = [_dot_tn(v, k * tl) for v, k, tl in zip(vs, ks, t["tail"])]
    st1s = [st0 * jnp.exp(ce) + u for st0, ce, u in zip(st0s, t["cend"], upd)]
    return os_, st1s


def _chunks_bwd(qs, ks, vs, gs, st0s, dos, dst1s, fws):
    n = len(qs)
    t = _chunks_terms(qs, ks, gs, fws)
    qcat, km, e, dirs = t["qcat"], t["km"], t["e"], t["dirs"]
    a_t = [jnp.where(d["causal_t"], _dot_nt(km_, qc), 0.0) for km_, qc, d in zip(km, qcat, dirs)]
    ktail = [k * tl for k, tl in zip(ks, t["tail"])]
    dv_a = [_dot(at, do) for at, do in zip(a_t, dos)]
    dv_b = [_dot_nt(kt, ds) for kt, ds in zip(ktail, dst1s)]
    dv = [x + y for x, y in zip(dv_a, dv_b)]
    da = [jnp.where(d["causal"], _dot_nt(do, v), 0.0) for do, v, d in zip(dos, vs, dirs)]
    da_t = [jnp.where(d["causal_t"], _dot_nt(v, do), 0.0) for do, v, d in zip(dos, vs, dirs)]
    dqcat = [_dot(da_, km_) for da_, km_ in zip(da, km)]
    dq_inter = [e[i][0] * _dot(dos[i], st0s[i]) for i in range(n)]
    dkm = [_dot(dat, qc) for dat, qc in zip(da_t, qcat)]
    dk_inter = [_dot(v, ds) * tl for v, ds, tl in zip(vs, dst1s, t["tail"])]
    dq = [dq_inter[i] + sum(e[i][j] * dqcat[i][:, j * HD:(j + 1) * HD] for j in range(NSUB)) for i in range(n)]
    dkt = [sum(jnp.where(dirs[i]["in_blk"][j], dkm[i][:, j * HD:(j + 1) * HD], 0.0) for j in range(NSUB))
           for i in range(n)]
    dk = [dkt[i] * t["kscale"][i] + dk_inter[i] for i in range(n)]
    dcum = [qs[i] * dq_inter[i] - ks[i] * dk_inter[i] - t["kt"][i].astype(BF16).astype(F32) * dkt[i]
            + sum(qcat[i][:, j * HD:(j + 1) * HD].astype(F32) * dqcat[i][:, j * HD:(j + 1) * HD] for j in range(NSUB))
            for i in range(n)]
    ecend = [jnp.exp(ce) for ce in t["cend"]]
    end = [ecend[i] * _colsum(st0s[i] * dst1s[i]) + _colsum(ks[i] * dk_inter[i]) for i in range(n)]
    sums = _running_sums(dcum, [not fw for fw in fws])
    dg = [sm + en for sm, en in zip(sums, end)]
    upd = [_dot_tn(dos[i], qs[i] * e[i][0]) for i in range(n)]
    dst0 = [dst1s[i] * ecend[i] + upd[i] for i in range(n)]
    return dq, dk, dv, dg, dst0


def _chunk_index(step, n_ctx_chunks, n_chunks, fw):
    if fw:
        return step
    return jnp.where(step < n_ctx_chunks, n_ctx_chunks - 1 - step, n_chunks - 1 + n_ctx_chunks - step)


def _hg_inputs(hq, hf, lbv, d_idx, sl):
    lb = _sigmoid(lbv[d_idx:d_idx + 1, sl] - lbv[2 + d_idx:3 + d_idx, sl])
    sg = _sigmoid(hf)
    f = lb + (1.0 - lb) * sg
    return _silu(hq), 1.0 - f, jnp.log(f), f, sg, lb


def _scan_fwd_both(p, branch_sides, n_ctx_chunks, name):
    rows = p.shape[0]
    n_chunks = rows // CHUNK
    n_ins = [4 if branch == "hg" else 6 for branch, _ in branch_sides]
    n_in_all = 2 * sum(n_ins)
    n_br = len(branch_sides)

    def body(*refs):
        ins, outs, state = refs[:n_in_all], refs[n_in_all:n_in_all + 4 * n_br], refs[-1]

        @pl.when(pl.program_id(0) == 0)
        def _():
            state[...] = jnp.zeros_like(state)

        lanes, where = [], []
        pos = 0
        for bi, (branch, _) in enumerate(branch_sides):
            hg = branch == "hg"
            n_in = n_ins[bi]
            for di, fw in enumerate((True, False)):
                r = ins[pos:pos + n_in]
                pos += n_in
                o_ref, st_ref = outs[4 * bi + 2 * di], outs[4 * bi + 2 * di + 1]
                if hg:
                    a_ref, b_ref, c_ref, lb_ref = r
                else:
                    a_ref, b_ref, c_ref, lr_ref, wgk_ref, bgk_ref = r
                    logits = _dot(lr_ref[...], wgk_ref[...]) + bgk_ref[...]
                    g_all = _log_sigmoid(logits) * (1.0 / GATE_NORM)
                for h in range(NH):
                    sl = slice(h * HD, (h + 1) * HD)
                    if hg:
                        q, k, g, _, _, _ = _hg_inputs(a_ref[:, sl], c_ref[:, sl], lb_ref[...], di, sl)
                        v = b_ref[:, sl]
                    else:
                        q, k, v, g = a_ref[:, sl] * (HD ** -0.5), b_ref[:, sl], c_ref[:, sl], g_all[:, sl]
                    lanes.append((q, k, v, g, state[2 * bi + di, h], fw))
                    where.append((2 * bi + di, h, sl, o_ref, st_ref))
        qs, ks, vs, gs, st0s, fws = (list(col) for col in zip(*lanes))
        os_, st1s = _chunks_fwd(qs, ks, vs, gs, st0s, fws)
        for (si, h, sl, o_ref, st_ref), st0, o, st1 in zip(where, st0s, os_, st1s):
            st_ref[0, h] = st0
            o_ref[:, sl] = o
            state[si, h] = st1

    fixed = lambda j: (0, 0)
    in_specs, args, out_specs = [], [], []
    for branch, side in branch_sides:
        for di, fw in enumerate((True, False)):
            chunk = functools.partial(_chunk_index, n_ctx_chunks=n_ctx_chunks, n_chunks=n_chunks, fw=fw)

            def cmap(blk, width=HW, chunk=chunk):
                return pl.BlockSpec((CHUNK, width), lambda j: (chunk(j), blk))

            if branch == "hg":
                in_specs += [cmap(C_HQ), cmap(C_HI), cmap(C_HF_FW + di), pl.BlockSpec((4, HW), fixed)]
                args += [p, p, p, side]
            else:
                in_specs += [cmap(C_GQ), cmap(C_GK), cmap(C_GV), cmap(OFF_LR // 128, 128),
                             pl.BlockSpec((128, HW), fixed), pl.BlockSpec((1, HW), fixed)]
                args += [p, p, p, p, side[di][0], side[di][1]]
            out_specs += [cmap(0), pl.BlockSpec((1, NH, HD, HD), lambda j, chunk=chunk: (chunk(j), 0, 0, 0))]
    return pl.pallas_call(
        body, name=name, grid=(n_chunks,),
        out_shape=[jax.ShapeDtypeStruct((rows, HW), F32),
                   jax.ShapeDtypeStruct((n_chunks, NH, HD, HD), F32)] * (2 * n_br),
        in_specs=in_specs, out_specs=out_specs,
        scratch_shapes=[pltpu.VMEM((2 * n_br, NH, HD, HD), F32)],
        compiler_params=_cparams(dimension_semantics=("arbitrary",)),
    )(*args)


def _scan_bwd_both(p, branch_items, n_ctx_chunks, name):
    rows = p.shape[0]
    n_chunks = rows // CHUNK
    n_ins = [6 if item[0] == "hg" else 8 for item in branch_items]
    n_outs = [4 if item[0] == "hg" else 6 for item in branch_items]
    n_in_all, n_out_all = 2 * sum(n_ins), 2 * sum(n_outs)

    def body(*refs):
        ins, outs, dstate = refs[:n_in_all], refs[n_in_all:n_in_all + n_out_all], refs[-1]
        first = pl.program_id(0) == 0

        @pl.when(first)
        def _():
            dstate[...] = jnp.zeros_like(dstate)

        lanes, where, extra, ctx = [], [], [], []
        ipos = opos = 0
        for bi, item in enumerate(branch_items):
            hg = item[0] == "hg"
            for di, fw in enumerate((True, False)):
                r, w = ins[ipos:ipos + n_ins[bi]], outs[opos:opos + n_outs[bi]]
                ipos += n_ins[bi]
                opos += n_outs[bi]
                if hg:
                    a_ref, b_ref, c_ref, lb_ref, st_ref, do_ref = r
                    acc_refs = (w[3],)
                else:
                    a_ref, b_ref, c_ref, lr_ref, wgk_ref, bgk_ref, st_ref, do_ref = r
                    acc_refs = (w[4], w[5])
                    lr = lr_ref[...]
                    logits = _dot(lr, wgk_ref[...]) + bgk_ref[...]
                    g_all = _log_sigmoid(logits) * (1.0 / GATE_NORM)

                @pl.when(first)
                def _(acc_refs=acc_refs):
                    for ref in acc_refs:
                        ref[...] = jnp.zeros_like(ref)

                for h in range(NH):
                    sl = slice(h * HD, (h + 1) * HD)
                    if hg:
                        hq, hf = a_ref[:, sl], c_ref[:, sl]
                        q, k, g, f, sg, lb = _hg_inputs(hq, hf, lb_ref[...], di, sl)
                        v = b_ref[:, sl]
                        extra.append((hq, f, sg, lb))
                    else:
                        q, k, v, g = a_ref[:, sl] * (HD ** -0.5), b_ref[:, sl], c_ref[:, sl], g_all[:, sl]
                        extra.append(None)
                    lanes.append((q, k, v, g, st_ref[0, h], do_ref[:, sl], dstate[2 * bi + di, h], fw))
                    where.append((2 * bi + di, h, sl))
                ctx.append((hg, w, None if hg else (lr, logits, wgk_ref)))

        dqs, dks, dvs, dgs, dst0s = [], [], [], [], []
        for lo in range(0, len(lanes), BWD_LANES):
            cols = [list(col) for col in zip(*lanes[lo:lo + BWD_LANES])]
            for acc, part in zip((dqs, dks, dvs, dgs, dst0s), _chunks_bwd(*cols)):
                acc.extend(part)
        dg_parts = [[] for _ in ctx]
        for (si, h, sl), ex, dq, dk, dv, dg, dst0 in zip(where, extra, dqs, dks, dvs, dgs, dst0s):
            dstate[si, h] = dst0
            hg, w, _ = ctx[si]
            if hg:
                hq, f, sg, lb = ex
                da_ref, db_ref, dc_ref, dlb_ref = w
                da_ref[:, sl] = (dq * _dsilu(hq)).astype(BF16)
                db_ref[:, sl] = dv.astype(BF16)
                df = dg / f - dk
                dc_ref[:, sl] = (df * (1.0 - lb) * sg * (1.0 - sg)).astype(BF16)
                dlb_ref[0:1, sl] += _colsum(df * (1.0 - sg))
            else:
                da_ref, db_ref, dc_ref = w[:3]
                da_ref[:, sl] = (dq * (HD ** -0.5)).astype(BF16)
                db_ref[:, sl] = dk.astype(BF16)
                dc_ref[:, sl] = dv.astype(BF16)
                dg_parts[si].append(dg)
        for si, (hg, w, more) in enumerate(ctx):
            if not hg:
                dlr_ref, dwgk_ref, dbias_ref = w[3:]
                lr, logits, wgk_ref = more
                dlogits = jnp.concatenate(dg_parts[si], axis=1) * (1.0 / GATE_NORM) * (1.0 - _sigmoid(logits))
                dlr_ref[...] = _dot_nt(dlogits, wgk_ref[...]).astype(BF16)
                dwgk_ref[...] += _dot_tn(lr, dlogits)
                dbias_ref[0:1, :] += _colsum(dlogits)

    fixed = lambda j: (0, 0)
    big = jax.ShapeDtypeStruct((rows, HW), BF16)
    in_specs, args, out_shape, out_specs = [], [], [], []
    for branch, side, states, d_o in branch_items:
        for di, fw in enumerate((True, False)):
            def chunk_of(j, fw=fw):
                return _chunk_index(n_chunks - 1 - j, n_ctx_chunks, n_chunks, fw)

            def cmap(blk, width=HW, chunk_of=chunk_of):
                return pl.BlockSpec((CHUNK, width), lambda j: (chunk_of(j), blk))

            st_spec = pl.BlockSpec((1, NH, HD, HD), lambda j, chunk_of=chunk_of: (chunk_of(j), 0, 0, 0))
            if branch == "hg":
                in_specs += [cmap(C_HQ), cmap(C_HI), cmap(C_HF_FW + di), pl.BlockSpec((4, HW), fixed), st_spec,
                             cmap(0)]
                args += [p, p, p, side, states[di], d_o]
                out_shape += [big, big, big, jax.ShapeDtypeStruct((8, HW), F32)]
                out_specs += [cmap(0), cmap(0), cmap(0), pl.BlockSpec((8, HW), fixed)]
            else:
                in_specs += [cmap(C_GQ), cmap(C_GK), cmap(C_GV), cmap(OFF_LR // 128, 128),
                             pl.BlockSpec((128, HW), fixed), pl.BlockSpec((1, HW), fixed), st_spec, cmap(0)]
                args += [p, p, p, p, side[di][0], side[di][1], states[di], d_o]
                out_shape += [big, big, big, jax.ShapeDtypeStruct((rows, 128), BF16),
                              jax.ShapeDtypeStruct((128, HW), F32), jax.ShapeDtypeStruct((8, HW), F32)]
                out_specs += [cmap(0), cmap(0), cmap(0), cmap(0, 128), pl.BlockSpec((128, HW), fixed),
                              pl.BlockSpec((8, HW), fixed)]
    return pl.pallas_call(
        body, name=name, grid=(n_chunks,), out_shape=out_shape, in_specs=in_specs, out_specs=out_specs,
        scratch_shapes=[pltpu.VMEM((2 * len(branch_items), NH, HD, HD), F32)],
        compiler_params=_cparams(dimension_semantics=("arbitrary",)),
    )(*args)


SMALL_ROWS = 56
ROWS_MOD_X = (0, 1, 8, 16, 17, 18)
ROWS_MOD_C = (2, 3)
ROW_PRE1, ROW_POST1, ROW_ONORM, ROW_PRE2, ROW_POST2, ROW_LB, ROW_BGK, ROW_WGK = 4, 9, 10, 19, 20, 24, 32, 40
ROW_LOSS = 21


def _reduce_small(gathered, lb_full, name, after=()):
    _, _, d = gathered.shape

    def body(g_ref, lb_ref, *rest):
        sum_ref, dmod_ref, dbmod_ref, dlb_ref = rest[len(after):]
        total = g_ref[0]
        for b in range(1, N_DEV):
            total = total + g_ref[b]
        sum_ref[...] = total
        dmod_ref[...] = jnp.zeros_like(dmod_ref)
        for m in range(N_MOD):
            col = slice(m * d, (m + 1) * d)
            acc = jnp.zeros((1, d), F32)
            for b in range(N_DEV):
                row = g_ref[b, ROWS_MOD_X[m]:ROWS_MOD_X[m] + 1, :]
                dmod_ref[b:b + 1, col] = row
                acc = acc + row
            if m < 2:
                ctx_row = total[ROWS_MOD_C[m]:ROWS_MOD_C[m] + 1, :]
                dmod_ref[8:9, col] = ctx_row
                acc = acc + ctx_row
            dbmod_ref[:, col] = acc
        lbv = lb_ref[...]
        for dd in range(2):
            lb = _sigmoid(lbv[dd:dd + 1, :] - lbv[2 + dd:3 + dd, :])
            gl = total[ROW_LB:ROW_LB + 1, dd * HW:(dd + 1) * HW] * lb * (1.0 - lb)
            dlb_ref[dd:dd + 1, :] = gl
            dlb_ref[2 + dd:3 + dd, :] = -gl

    return pl.pallas_call(
        body, name=name,
        out_shape=[jax.ShapeDtypeStruct((SMALL_ROWS, d), F32), jax.ShapeDtypeStruct((16, N_MOD * d), F32),
                   jax.ShapeDtypeStruct((1, N_MOD * d), F32), jax.ShapeDtypeStruct((4, HW), F32)],
        in_specs=[VMEM_SPEC] * 2 + [ANY_SPEC] * len(after), out_specs=[VMEM_SPEC] * 4, compiler_params=_cparams(),
    )(gathered, lb_full, *after)


def _c_ctx_grad(gathered, c_ctx_row, name):
    def body(g_ref, c_ref, o_ref):
        acc = g_ref[0, 0:1, :]
        for chip in range(1, N_CHIP):
            acc = acc + g_ref[2 * chip, 0:1, :]
        o_ref[...] = acc * _dsilu(c_ref[...])

    return pl.pallas_call(
        body, name=name, out_shape=jax.ShapeDtypeStruct(c_ctx_row.shape, F32),
        in_specs=[VMEM_SPEC] * 2, out_specs=VMEM_SPEC, compiler_params=_cparams(),
    )(gathered, c_ctx_row)


def _blocked(full, n_blocks):
    k, n = full.shape
    return full.reshape(k, n_blocks, n // n_blocks).transpose(1, 0, 2)


def _unblocked(blocks):
    nb, k, n = blocks.shape
    return blocks.transpose(1, 0, 2).reshape(k, nb * n)


def _sample_front(x0, ctx0, modc, modx, norm_pre1, lb_full, gla_side, w_in_r):
    ctx_len = ctx0.shape[0]
    n_ctx_tiles = ctx_len // TM
    n_ctx_chunks = ctx_len // CHUNK
    h1, p = _in_projection(ctx0, x0, modc, modx, norm_pre1, w_in_r, n_ctx_tiles, "in_projection")
    (o_hg_fw, st_hg_fw, o_hg_bw, st_hg_bw, o_gla_fw, st_gla_fw, o_gla_bw, st_gla_bw) = _scan_fwd_both(
        p, [("hg", lb_full), ("gla", gla_side)], n_ctx_chunks, "scan_fwd")
    return dict(h1=h1, p=p, o_list=[o_hg_fw, o_hg_bw, o_gla_fw, o_gla_bw],
                states=[st_hg_fw, st_hg_bw, st_gla_fw, st_gla_bw])


def _sample_back(reduce, front, x0, ctx0, target0, modc, modx, norm_pre1, norms, onorms, lb_full, gla_side, w_in_r,
                 wbh, wbg, wout, ffn_weights):
    seq, d = x0.shape
    ctx_len = ctx0.shape[0]
    n_ctx_tiles = ctx_len // TM
    n_tiles = (ctx_len + seq) // TM
    n_ctx_chunks = ctx_len // CHUNK
    h1, p, o_list = front["h1"], front["p"], front["o_list"]
    st_hg_fw, st_hg_bw, st_gla_fw, st_gla_bw = front["states"]
    z2, y1, merged, og_hg, og_gla = _mixer_tail_fwd(x0, p, o_list, modx, norms, onorms, wbh, wbg, wout, n_ctx_tiles,
                                                    "mixer_tail")
    wg, wu, wd = ffn_weights([z2])
    loss_part, dz2, h2, a_act, du, dv, dy2, stat_ffn = _ffn_fwd_bwd(z2, modx, norms, wg, wu, wd, target0, "ffn")
    dff = wg.shape[0]
    tok = reduce("ffn", [_weight_grad(du, h2, "grad_w_ff_gate", tk=dff // 2, tn=d),
                         _weight_grad(dv, h2, "grad_w_ff_up", tk=dff // 2, tn=d),
                         _weight_grad(a_act, dy2, "grad_w_ff_down", tk=dff // 2)])

    (d_ohg, d_ogla, d_hgate, d_ggate, d_ghg, d_ggla, dy1, db_hg, db_gla, stat_mix) = _mixer_tail_bwd(
        x0, p, o_list, dz2, y1, modx + tok, norms, onorms, wbh, wbg, wout, n_ctx_tiles, n_tiles, "mixer_tail_bwd")
    tok = reduce("mix", [_weight_grad(og_hg, db_hg, "grad_w_br_hg"), _weight_grad(og_gla, db_gla, "grad_w_br_gla"),
                         _weight_grad(merged, dy1, "grad_w_out")])
    tok = tok + reduce("push_ffn", [dy1])
    gla_b = [(wgk, bias + tok) for wgk, bias in gla_side]
    (dgq_f, dgk_f, dgv_f, dlr_f, dwgk_f, dbgk_f, dgq_b, dgk_b, dgv_b, dlr_b, dwgk_b, dbgk_b,
     dhq_f, dhi_f, dhf_f, dlb_f, dhq_b, dhi_b, dhf_b, dlb_b) = _scan_bwd_both(
        p, [("gla", gla_b, (st_gla_fw, st_gla_bw), d_ogla), ("hg", lb_full, (st_hg_fw, st_hg_bw), d_ohg)],
        n_ctx_chunks, "scan_bwd")
    tok = reduce("push_mix", [dbgk_f])
    pieces = [dhq_f, dhq_b, dhi_f, dhi_b, dhf_f, dhf_b, d_hgate, dgq_f, dgq_b, dgk_f, dgk_b, dgv_f, dgv_b, d_ggate,
              d_ghg, d_ggla, dlr_f, dlr_b]
    dp, grad_x, stat_in = _in_projection_bwd(ctx0, x0, dz2, modc, modx, norm_pre1 + tok, w_in_r, pieces, n_ctx_tiles,
                                             "in_projection_bwd")

    started = reduce("small_start", dict(stat_in=stat_in, stat_mix=stat_mix, stat_ffn=stat_ffn, dlb=(dlb_f, dlb_b),
                                         dwgk=(dwgk_f, dwgk_b), dbgk=(dbgk_f, dbgk_b)))
    reduce("in", [_w_in_grad(dp, h1, w_in_r.shape[0], "grad_w_in", after=[started])])
    reduce("small", None)
    reduce("push_in", [])
    return dict(loss_part=loss_part, grad_x=grad_x)


def kernel(x, c, ctx, c_ctx, w_mod, b_mod, norm_pre1, norm_post1, norm_pre2, norm_post2, w_in, hg_lb, hg_onorm, gla_w_gk, gla_b_gk, gla_onorm, w_br_hg, w_br_gla, w_out, w_ff_gate, w_ff_up, w_ff_down, loss_target, m_c_ctx, m_w_mod, m_b_mod, m_norm_pre1, m_norm_post1, m_norm_pre2, m_norm_post2, m_w_in, m_hg_lb, m_hg_onorm, m_gla_w_gk, m_gla_b_gk, m_gla_onorm, m_w_br_hg, m_w_br_gla, m_w_out, m_w_ff_gate, m_w_ff_up, m_w_ff_down, v_c_ctx, v_w_mod, v_b_mod, v_norm_pre1, v_norm_post1, v_norm_pre2, v_norm_post2, v_w_in, v_hg_lb, v_hg_onorm, v_gla_w_gk, v_gla_b_gk, v_gla_onorm, v_w_br_hg, v_w_br_gla, v_w_out, v_w_ff_gate, v_w_ff_up, v_w_ff_down):
    seq, d = x.shape[1], x.shape[2]
    ctx_len = ctx.shape[1]
    assert seq % TM == 0 and ctx_len % TM == 0 and d == 2 * HW
    ax, ay, ac = lax.axis_index("x"), lax.axis_index("y"), lax.axis_index("c")
    chip = 2 * ax + ay
    dev = 2 * chip + ac
    c_arr = jnp.reshape(ac, (1,)).astype(jnp.int32)
    chip_arr = jnp.reshape(chip, (1,)).astype(jnp.int32)
    transposed = ("w_in", "w_ff_gate", "w_ff_up")
    view = lambda a, nm: a[0].T if nm in transposed else a[0]

    sems_in, lands_in, token_in0 = _blocks_start([_cast_into_blocks(chip_arr, view(w_in, "w_in"), "cast_w_in")],
                                                 "gather_w_in_start")

    nc = d // 128
    pad8 = lambda a: jnp.pad(a, ((0, -a.shape[0] % 8), (0, 0)))
    small1 = jnp.concatenate([c.reshape(nc, 128) + token_in0[0, 0], pad8(hg_lb.reshape(4, 128)),
                              gla_w_gk.reshape(2 * RANK, 128), pad8(gla_b_gk.reshape(2, 128))], axis=0)
    blocks = [_cast_into_blocks(chip_arr, view(w_, nm), "cast_" + nm) for w_, nm in (
        (w_br_hg, "w_br_hg"), (w_br_gla, "w_br_gla"), (w_out, "w_out"), (w_ff_gate, "w_ff_gate"),
        (w_ff_up, "w_ff_up"), (w_ff_down, "w_ff_down"))]
    got1 = _allgather8(small1, "gather_small_params", after=blocks)
    c_all = got1[:, :nc, :].reshape(N_DEV, d)
    per_chip = got1[0::2]
    lb_full = per_chip[:, nc:nc + 4, :].transpose(1, 0, 2).reshape(4, HW)
    wgk_full = per_chip[:, nc + 8:nc + 8 + 2 * RANK, :].transpose(1, 0, 2).reshape(2, RANK, HW)
    bgk_full = per_chip[:, nc + 8 + 2 * RANK:nc + 10 + 2 * RANK, :].transpose(1, 0, 2).reshape(2, HW)
    wgk_pad = [jnp.zeros((128, HW), F32).at[dd * RANK:(dd + 1) * RANK].set(wgk_full[dd]) for dd in range(2)]
    bgk = [bgk_full[dd:dd + 1] for dd in range(2)]

    n_mod_cols = w_mod.shape[2]
    cond = jnp.concatenate([c_all, pad8(c_ctx.reshape(1, d))], axis=0)
    b_cols = lax.dynamic_slice(b_mod, (0, chip * n_mod_cols), (1, n_mod_cols))
    lands_in = _blocks_wait(sems_in, lands_in, [got1], "gather_w_in_wait")
    fwd_sems, lands_in, fwd_token = _forward_start(lands_in, "gather_w_in_forward_start")
    mod_part = _mod_forward(cond + fwd_token[0, 0], w_mod[0], b_cols, "mod_forward")
    mod_got = _allgather8(mod_part, "gather_mod")
    mod_all = mod_got[0::2].transpose(1, 0, 2).reshape(16, N_CHIP * n_mod_cols)
    modx = pad8(lax.dynamic_slice(mod_all, (dev, 0), (1, N_MOD * d)).reshape(N_MOD, d))
    modc = pad8(mod_all[8].reshape(N_MOD, d))

    gathered_in = _forward_wait(fwd_sems, lands_in, [mod_got], "gather_w_in_forward_wait")
    sems, lands, token = _blocks_start(blocks, "gather_rest_start", after=[gathered_in[0]])
    w_in_r = gathered_in[0].reshape(-1, d)

    norms = jnp.concatenate([norm_pre1, norm_post1, norm_pre2, norm_post2, jnp.zeros((4, d), F32)], axis=0)
    onorms = jnp.zeros((8, d), F32).at[0, :HD].set(hg_onorm[0]).at[1, :HD].set(gla_onorm[0])
    gla_side = [(wgk_pad[dd], bgk[dd]) for dd in range(2)]
    modx = modx + token[0, 0]
    front = _sample_front(x[0], ctx[0], modc, modx, norm_pre1, lb_full, gla_side, w_in_r)
    lands = _blocks_wait(sems, lands, front["o_list"], "gather_rest_wait")
    gathered = _blocks_finish(lands[:3], "gather_mix_finish")
    wbh, wbg = _unblocked(gathered[0]), _unblocked(gathered[1])
    wout = gathered[2].reshape(d, d)
    ffn_sems, ffn_lands, ffn_token = _forward_start(lands[3:], "gather_ffn_forward_start")
    onorms = onorms + ffn_token[0, 0]

    def ffn_weights(after):
        got = _forward_wait(ffn_sems, ffn_lands, after, "gather_ffn_forward_wait")
        return tuple(g.reshape(-1, d) for g in got)

    dff = w_ff_down.shape[1] * N_CHIP
    groups = {"ffn": ["w_ff_gate", "w_ff_up", "w_ff_down"], "mix": ["w_br_hg", "w_br_gla", "w_out"], "in": ["w_in"]}
    row_sharded = {"w_out": d // N_CHIP, "w_ff_down": dff // N_CHIP, "w_ff_gate": dff // N_CHIP,
                   "w_ff_up": dff // N_CHIP, "w_in": w_in.shape[2]}
    in_flight, to_sibling, small = {}, {}, {}

    def reduce_small_start(stats):
        small2 = jnp.concatenate([
            stats["stat_in"], stats["stat_mix"], stats["stat_ffn"],
            jnp.concatenate(stats["dlb"], axis=1), jnp.concatenate(stats["dbgk"], axis=1),
            jnp.concatenate([stats["dwgk"][0][0:RANK], stats["dwgk"][1][RANK:2 * RANK]], axis=1)], axis=0)
        assert small2.shape[0] == SMALL_ROWS
        sems_, land_, token_ = _allgather8_start(small2, "gather_small_grads_start")
        small.update(gathering=(sems_, land_))
        return token_

    def reduce_small():
        got2 = small["gathered"]
        total, dmod_all, g_b_mod, g_lb_full = _reduce_small(got2, lb_full, "reduce_small", to_sibling["in"][1])
        dmod_cols = lax.dynamic_slice(dmod_all, (0, chip * n_mod_cols), (16, n_mod_cols))
        g_w_mod, cctx_part = _mod_backward(cond, w_mod[0], dmod_cols, "mod_backward")
        got3 = _allgather8(cctx_part, "gather_c_ctx_grad")
        g_c_ctx = _c_ctx_grad(got3, c_ctx.reshape(1, d), "c_ctx_grad")
        small.update(total=total, g_b_mod=g_b_mod, g_lb_full=g_lb_full, g_w_mod=g_w_mod, g_c_ctx=g_c_ctx)

    def reduce(group, grads):
        if group == "small_start":
            return reduce_small_start(grads)
        if group == "small":
            return reduce_small()
        if group.startswith("push_"):
            return push(group[5:], grads)
        nms = groups[group]
        after = []
        if group == "in":
            small.update(gathered=_allgather8_wait(*small["gathering"], grads, "gather_small_grads_wait"))
            after = [small["gathered"]]
        full = [g.reshape(N_CHIP, row_sharded[nm], d) if nm in row_sharded else _blocked(g, N_CHIP)
                for g, nm in zip(grads, nms)]
        sems_, full, lands_, token_ = _send_half_start(full, "grads_to_sibling_start_" + group, after)
        to_sibling[group] = (sems_, full, lands_)
        return token_[0, 0]

    def push(group, after):
        nms = groups[group]
        sems_, full, lands_ = to_sibling[group]
        if group == "in":
            after = list(after) + [small["g_c_ctx"], small["total"]]
        full, from_sibling = _send_half_wait(sems_, full, lands_, after, "grads_to_sibling_wait_" + group)
        pairs = [_pair_sum(c_arr, f, r_, "pair_sum_" + nm) for f, r_, nm in zip(full, from_sibling, nms)]
        after = [small["g_c_ctx"], small["total"]] if group == "in" else []
        sems_, pairs, lands_, token_ = _scatter_start(pairs, "grads_to_owner_start_" + group, after)
        in_flight[group] = (sems_, pairs, lands_, token_)
        return token_[0, 0]

    r = _sample_back(reduce, front, x[0], ctx[0], loss_target[0], modc, modx, norm_pre1, norms, onorms, lb_full,
                     gla_side, w_in_r, wbh, wbg, wout, ffn_weights)
    grad_x = r["grad_x"]

    weights = dict(w_in=(w_in, m_w_in, v_w_in), w_br_hg=(w_br_hg, m_w_br_hg, v_w_br_hg),
                   w_br_gla=(w_br_gla, m_w_br_gla, v_w_br_gla), w_out=(w_out, m_w_out, v_w_out),
                   w_ff_gate=(w_ff_gate, m_w_ff_gate, v_w_ff_gate), w_ff_up=(w_ff_up, m_w_ff_up, v_w_ff_up),
                   w_ff_down=(w_ff_down, m_w_ff_down, v_w_ff_down))
    names = ["w_in", "w_br_hg", "w_br_gla", "w_out", "w_ff_gate", "w_ff_up", "w_ff_down"]
    big, swapping = {}, {}

    def sum_and_swap(group, after):
        sems_, pairs, lands_, _ = in_flight[group]
        pairs, lands_ = _scatter_wait(sems_, pairs, lands_, after, "grads_to_owner_wait_" + group)
        own_half = [_sum_owner(chip_arr, pr, g, "chip_sum_" + nm) for pr, g, nm in zip(pairs, lands_, groups[group])]
        swapping[group] = _swap_start(own_half, "halves_to_sibling_start_" + group)
        return own_half[-1]

    def update(group, after):
        sems_, own_half, lands_ = swapping[group]
        own_half, other_half = _swap_wait(sems_, own_half, lands_, after, "halves_to_sibling_wait_" + group)
        done = []
        for nm, own, oth in zip(groups[group], own_half, other_half):
            w_, m_, v_ = (view(a, nm) for a in weights[nm])
            res = _adamw_halves(c_arr, own, oth, w_, m_, v_, "adamw_" + nm)
            big[nm] = [r_.T[None] if nm in transposed else r_[None] for r_ in res]
            done.append(res[1])
        return done

    token_in = in_flight["in"][3]
    summed_ffn = sum_and_swap("ffn", [token_in])
    summed_mix = sum_and_swap("mix", [summed_ffn])

    total, g_b_mod, g_lb_full, g_w_mod, g_c_ctx = (small[k] for k in ("total", "g_b_mod", "g_lb_full", "g_w_mod",
                                                                      "g_c_ctx"))
    g_pre1, g_post1, g_pre2, g_post2 = (total[r_:r_ + 1] for r_ in (ROW_PRE1, ROW_POST1, ROW_PRE2, ROW_POST2))
    g_hg_on, g_gla_on = total[ROW_ONORM:ROW_ONORM + 1, 0:HD], total[ROW_ONORM:ROW_ONORM + 1, HD:2 * HD]
    n_lb = hg_lb.shape[2]
    g_hg_lb = lax.dynamic_slice(g_lb_full, (0, chip * n_lb), (4, n_lb))
    g_bgk = lax.dynamic_slice(total[ROW_BGK:ROW_BGK + 1].reshape(2, HW), (0, chip * n_lb), (2, n_lb))
    g_wgk_full = total[ROW_WGK:ROW_WGK + RANK].reshape(RANK, 2, HW).transpose(1, 0, 2).reshape(2 * RANK, HW)
    g_wgk = lax.dynamic_slice(g_wgk_full, (0, chip * n_lb), (2 * RANK, n_lb))

    small_items = [
        (g_c_ctx, c_ctx.reshape(1, d), m_c_ctx.reshape(1, d), v_c_ctx.reshape(1, d)),
        (g_b_mod, b_mod, m_b_mod, v_b_mod),
        (g_pre1, norm_pre1, m_norm_pre1, v_norm_pre1),
        (g_post1, norm_post1, m_norm_post1, v_norm_post1),
        (g_pre2, norm_pre2, m_norm_pre2, v_norm_pre2),
        (g_post2, norm_post2, m_norm_post2, v_norm_post2),
        (g_hg_lb, hg_lb.reshape(4, n_lb), m_hg_lb.reshape(4, n_lb), v_hg_lb.reshape(4, n_lb)),
        (g_hg_on, hg_onorm, m_hg_onorm, v_hg_onorm),
        (g_wgk, gla_w_gk.reshape(2 * RANK, n_lb), m_gla_w_gk.reshape(2 * RANK, n_lb), v_gla_w_gk.reshape(2 * RANK, n_lb)),
        (g_bgk, gla_b_gk.reshape(2, n_lb), m_gla_b_gk.reshape(2, n_lb), v_gla_b_gk.reshape(2, n_lb)),
        (g_gla_on, gla_onorm, m_gla_onorm, v_gla_onorm),
    ]
    small_res = _adamw_whole(small_items, "adamw_small")
    mod_res = _adamw_tiled(g_w_mod, w_mod[0], m_w_mod[0], v_w_mod[0], "adamw_w_mod")
    done_ffn = update("ffn", [summed_mix, mod_res[0], small_res[0][0]])
    done_mix = update("mix", done_ffn)
    update("in", [sum_and_swap("in", done_mix)])

    loss = total[ROW_LOSS, 0]

    shapes = dict(c_ctx=c_ctx.shape, b_mod=b_mod.shape, norm_pre1=norm_pre1.shape, norm_post1=norm_post1.shape,
                  norm_pre2=norm_pre2.shape, norm_post2=norm_post2.shape, hg_lb=hg_lb.shape, hg_onorm=hg_onorm.shape,
                  gla_w_gk=gla_w_gk.shape, gla_b_gk=gla_b_gk.shape, gla_onorm=gla_onorm.shape)
    small_names = ["c_ctx", "b_mod", "norm_pre1", "norm_post1", "norm_pre2", "norm_post2", "hg_lb", "hg_onorm",
                   "gla_w_gk", "gla_b_gk", "gla_onorm"]
    grads, deltas, new_m, new_v = {}, {}, {}, {}
    for nm, item, res in zip(small_names, small_items, small_res):
        grads[nm] = item[0].reshape(shapes[nm])
        deltas[nm], new_m[nm], new_v[nm] = (r_.reshape(shapes[nm]) for r_ in res)
    grads["w_mod"] = g_w_mod[None]
    deltas["w_mod"], new_m["w_mod"], new_v["w_mod"] = (r_[None] for r_ in mod_res)
    for nm in names:
        grads[nm], deltas[nm], new_m[nm], new_v[nm] = big[nm]
    order = ["c_ctx", "w_mod", "b_mod", "norm_pre1", "norm_post1", "norm_pre2", "norm_post2", "w_in", "hg_lb",
             "hg_onorm", "gla_w_gk", "gla_b_gk", "gla_onorm", "w_br_hg", "w_br_gla", "w_out", "w_ff_gate", "w_ff_up",
             "w_ff_down"]
    return (loss, grad_x[None], *[grads[n] for n in order], *[deltas[n] for n in order],
            *[new_m[n] for n in order], *[new_v[n] for n in order])
```

```python
import functools

import jax
import jax.numpy as jnp
from jax import lax
from jax.experimental import pallas as pl
from jax.experimental.pallas import tpu as pltpu

F32 = jnp.float32
BF16 = jnp.bfloat16
MESH = pl.DeviceIdType.MESH

EPS = 1e-6
CHUNK = 64
SUB = 16
NSUB = CHUNK // SUB
NH = 4
HD = 128
HW = NH * HD
RANK = 16
GATE_NORM = 16.0
N_MOD = 6
TM = 256
BWD_LANES = 8
N_DEV = 8
N_CHIP = 4
VMEM_LIMIT = 56 * 1024 * 1024

ADAM_LR = 0.001
ADAM_B1 = 0.9
ADAM_B2 = 0.999
ADAM_EPS = 1e-08
ADAM_WD = 0.01
ADAM_STEP = 10

VMEM_SPEC = pl.BlockSpec(memory_space=pltpu.VMEM)
ANY_SPEC = pl.BlockSpec(memory_space=pl.ANY)
HBM_SPEC = pl.BlockSpec(memory_space=pltpu.HBM)
SEM_SPEC = pl.BlockSpec(memory_space=pltpu.SEMAPHORE)
EFFECT = pltpu.SideEffectType.DATAFLOW_SIDE_EFFECTING


def _cparams(**kw):
    return pltpu.CompilerParams(vmem_limit_bytes=VMEM_LIMIT, **kw)


def _dot(a, b):
    return jnp.dot(a.astype(BF16), b.astype(BF16), preferred_element_type=F32)


def _dot_nt(a, b):
    return lax.dot_general(a.astype(BF16), b.astype(BF16), (((1,), (1,)), ((), ())), preferred_element_type=F32)


def _dot_tn(a, b):
    return lax.dot_general(a.astype(BF16), b.astype(BF16), (((0,), (0,)), ((), ())), preferred_element_type=F32)


def _sigmoid(x):
    return 1.0 / (1.0 + jnp.exp(-x))


def _silu(x):
    return x * _sigmoid(x)


def _dsilu(x):
    s = _sigmoid(x)
    return s * (1.0 + x * (1.0 - s))


def _log_sigmoid(x):
    return jnp.minimum(x, 0.0) - jnp.log(1.0 + jnp.exp(-jnp.abs(x)))


def _colsum(a):
    return jnp.sum(a, axis=0, keepdims=True)


def _rms(a):
    r = lax.rsqrt(jnp.mean(a * a, axis=-1, keepdims=True) + EPS)
    return a * r, r


def _rms_bwd(dn, n, r):
    return r * (dn - n * jnp.mean(dn * n, axis=-1, keepdims=True))


def _place():
    x, y, c = lax.axis_index("x"), lax.axis_index("y"), lax.axis_index("c")
    chips = [(1 - x, y), (x, 1 - y), (1 - x, 1 - y)]
    return x, y, c, chips


def _allgather8(v, name, after=()):
    rows, cols = v.shape
    n_after = len(after)

    def body(x_ref, *rest):
        out_ref, send_sems, recv_sems, local_sem = rest[n_after:]
        x, y, c, chips = _place()
        me, sibling = (x, y, c), (x, y, 1 - c)

        def blk(px, py, pc):
            return out_ref.at[4 * px + 2 * py + pc]

        def copy(k, block, to, src=None):
            return pltpu.make_async_remote_copy(
                src_ref=blk(*block) if src is None else src, dst_ref=blk(*block),
                send_sem=send_sems.at[k], recv_sem=recv_sems.at[k], device_id=to, device_id_type=MESH)

        mine = pltpu.make_async_copy(x_ref, blk(*me), local_sem)
        mine.start()
        first = [copy(0, me, sibling, src=x_ref)]
        first += [copy(1 + j, me, (*chip, c), src=x_ref) for j, chip in enumerate(chips)]
        for cp in first:
            cp.start()
        passed = [copy(4 + j, (*chip, c), sibling) for j, chip in enumerate(chips)]
        for j, chip in enumerate(chips):
            copy(1 + j, (*chip, c), me).wait_recv()
            passed[j].start()
        copy(0, sibling, me).wait_recv()
        for j, chip in enumerate(chips):
            copy(4 + j, (*chip, 1 - c), me).wait_recv()
        for cp in first + passed:
            cp.wait_send()
        mine.wait()

    return pl.pallas_call(
        body, name=name,
        out_shape=jax.ShapeDtypeStruct((N_DEV, rows, cols), v.dtype),
        in_specs=[VMEM_SPEC] + [ANY_SPEC] * n_after, out_specs=VMEM_SPEC,
        scratch_shapes=[pltpu.SemaphoreType.DMA((7,)), pltpu.SemaphoreType.DMA((7,)), pltpu.SemaphoreType.DMA],
    )(v, *after)


def _allgather8_start(v, name):
    rows, cols = v.shape

    def body(x_ref, out_ref, *rest):
        send_sems, recv_sems, token, local_sem = rest[:4], rest[4:8], rest[-2], rest[-1]
        x, y, c, chips = _place()
        own = out_ref.at[4 * x + 2 * y + c]
        mine = pltpu.make_async_copy(x_ref, own, local_sem)
        mine.start()
        mine.wait()
        for k, to in enumerate([(x, y, 1 - c)] + [(*chip, c) for chip in chips]):
            pltpu.make_async_remote_copy(src_ref=own, dst_ref=own, send_sem=send_sems[k], recv_sem=recv_sems[k],
                                         device_id=to, device_id_type=MESH).start()
        token[...] = jnp.zeros_like(token)

    land = _hbm(lax.empty((N_DEV, rows, cols), v.dtype))
    out = pl.pallas_call(
        body, name=name,
        out_shape=(*[pltpu.SemaphoreType.DMA(())] * 8, pltpu.HBM(land.shape, land.dtype),
                   jax.ShapeDtypeStruct((8, 128), F32)),
        in_specs=[VMEM_SPEC, HBM_SPEC],
        out_specs=(*[SEM_SPEC] * 8, HBM_SPEC, VMEM_SPEC),
        input_output_aliases={1: 8},
        scratch_shapes=[pltpu.SemaphoreType.DMA],
        compiler_params=pltpu.CompilerParams(has_side_effects=EFFECT),
    )(v, land)
    return list(out[:8]), out[8], out[9]


def _allgather8_wait(sems, land, after, name):
    def blk(ref, px, py, pc):
        return ref.at[4 * px + 2 * py + pc]

    def wait_body(out_ref, *rest):
        send_sems, recv_sems = rest[:4], rest[4:8]
        x, y, c, chips = _place()
        me = (x, y, c)
        for k, peer in enumerate([(x, y, 1 - c)] + [(*chip, c) for chip in chips]):
            sent = pltpu.make_async_remote_copy(
                src_ref=blk(out_ref, *me), dst_ref=blk(out_ref, *me), send_sem=send_sems[k], recv_sem=recv_sems[k],
                device_id=peer, device_id_type=MESH)
            sent.wait_send()
            pltpu.make_async_remote_copy(
                src_ref=blk(out_ref, *peer), dst_ref=blk(out_ref, *peer), send_sem=send_sems[k],
                recv_sem=recv_sems[k], device_id=me, device_id_type=MESH).wait_recv()

    def pass_body(out_ref, _, send_sems, recv_sems):
        x, y, c, chips = _place()
        started = []
        for j, chip in enumerate(chips):
            cp = pltpu.make_async_remote_copy(
                src_ref=blk(out_ref, *chip, c), dst_ref=blk(out_ref, *chip, c), send_sem=send_sems.at[j],
                recv_sem=recv_sems.at[j], device_id=(x, y, 1 - c), device_id_type=MESH)
            cp.start()
            started.append(cp)
        for j, chip in enumerate(chips):
            pltpu.make_async_remote_copy(
                src_ref=blk(out_ref, *chip, 1 - c), dst_ref=blk(out_ref, *chip, 1 - c), send_sem=send_sems.at[j],
                recv_sem=recv_sems.at[j], device_id=(x, y, c), device_id_type=MESH).wait_recv()
        for cp in started:
            cp.wait_send()

    land = pl.pallas_call(
        wait_body, name=name,
        out_shape=pltpu.HBM(land.shape, land.dtype),
        in_specs=[HBM_SPEC] + [SEM_SPEC] * 8 + [ANY_SPEC] * len(after),
        out_specs=HBM_SPEC,
        input_output_aliases={0: 0},
        compiler_params=pltpu.CompilerParams(has_side_effects=EFFECT),
    )(land, *sems, *after)
    return pl.pallas_call(
        pass_body, name=name + "_pass",
        out_shape=jax.ShapeDtypeStruct(land.shape, land.dtype),
        in_specs=[ANY_SPEC], out_specs=ANY_SPEC,
        input_output_aliases={0: 0},
        scratch_shapes=[pltpu.SemaphoreType.DMA((3,)), pltpu.SemaphoreType.DMA((3,))],
    )(land)


def _cast_into_blocks(chip_arr, w, name):
    rows, cols = w.shape
    tr = _row_tile(rows, 16, 256)

    def body(chip_ref, w_ref, o_ref):
        o_ref[0] = w_ref[...].astype(BF16)

    return pl.pallas_call(
        body, name=name,
        grid_spec=pltpu.PrefetchScalarGridSpec(
            num_scalar_prefetch=1, grid=(rows // tr,),
            in_specs=[pl.BlockSpec((tr, cols), lambda i, chip_ref: (i, 0))],
            out_specs=pl.BlockSpec((1, tr, cols), lambda i, chip_ref: (chip_ref[0], i, 0))),
        out_shape=jax.ShapeDtypeStruct((N_CHIP, rows, cols), BF16),
        compiler_params=_cparams(dimension_semantics=("parallel",)),
    )(chip_arr, w)


def _halved_by_rows(shape):
    return (shape[1] // 2) % 16 == 0


def _half_of(ref, pc, block=None):
    lead = slice(None) if block is None else block
    if _halved_by_rows(ref.shape):
        h = ref.shape[1] // 2
        return ref.at[lead, pl.ds(pl.multiple_of(pc * h, 16), h), :]
    h = ref.shape[2] // 2
    return ref.at[lead, :, pl.ds(pl.multiple_of(pc * h, 128), h)]


def _half_shape(shape):
    return (shape[0], shape[1] // 2, shape[2]) if _halved_by_rows(shape) else (shape[0], shape[1], shape[2] // 2)


def _half_rows(ref, chip_id, pc):
    return _half_of(ref, pc, chip_id)


def _hbm(a):
    return pltpu.with_memory_space_constraint(a, pltpu.HBM)


def _blocks_start(lands, name, after=()):
    n = len(lands)
    n_sem = 3 * n
    first = n + len(after)

    def body(*refs):
        lnd = refs[:n]
        send_sems, recv_sems = refs[first:first + n_sem], refs[first + n_sem:first + 2 * n_sem]
        token = refs[-1]
        x, y, c, chips = _place()
        me_chip = 2 * x + y
        for k in range(n):
            for j, chip in enumerate(chips):
                pltpu.make_async_remote_copy(
                    src_ref=_half_rows(lnd[k], me_chip, c), dst_ref=_half_rows(lnd[k], me_chip, c),
                    send_sem=send_sems[3 * k + j], recv_sem=recv_sems[3 * k + j],
                    device_id=(*chip, c), device_id_type=MESH).start()
        token[...] = jnp.zeros_like(token)

    out = pl.pallas_call(
        body, name=name,
        out_shape=(*[pltpu.SemaphoreType.DMA(())] * (2 * n_sem),
                   *[pltpu.HBM(l.shape, l.dtype) for l in lands],
                   jax.ShapeDtypeStruct((8, 128), F32)),
        in_specs=[HBM_SPEC] * n + [ANY_SPEC] * len(after),
        out_specs=(*[SEM_SPEC] * (2 * n_sem), *[HBM_SPEC] * n, VMEM_SPEC),
        input_output_aliases={i: 2 * n_sem + i for i in range(n)},
        compiler_params=pltpu.CompilerParams(has_side_effects=EFFECT),
    )(*[_hbm(l) for l in lands], *after)
    return list(out[:2 * n_sem]), list(out[2 * n_sem:2 * n_sem + n]), out[-1]


def _blocks_wait(sems, lands, after, name):
    n = len(lands)
    n_sem = 3 * n

    def body(*refs):
        lnd = refs[:n]
        s_sems, r_sems = refs[n:n + n_sem], refs[n + n_sem:n + 2 * n_sem]
        x, y, c, chips = _place()
        me_chip = 2 * x + y
        for k in range(n):
            for j, (px, py) in enumerate(chips):
                cp = pltpu.make_async_remote_copy(
                    src_ref=_half_rows(lnd[k], me_chip, c), dst_ref=_half_rows(lnd[k], 2 * px + py, c),
                    send_sem=s_sems[3 * k + j], recv_sem=r_sems[3 * k + j],
                    device_id=(px, py, c), device_id_type=MESH)
                cp.wait_send()
                cp.wait_recv()

    out = pl.pallas_call(
        body, name=name,
        out_shape=tuple(pltpu.HBM(l.shape, l.dtype) for l in lands),
        in_specs=[HBM_SPEC] * n + [SEM_SPEC] * (2 * n_sem) + [ANY_SPEC] * len(after),
        out_specs=[HBM_SPEC] * n,
        input_output_aliases={i: i for i in range(n)},
        compiler_params=pltpu.CompilerParams(has_side_effects=EFFECT),
    )(*lands, *sems, *after)
    return list(out)


def _forward_start(lands, name):
    n = len(lands)
    n_sem = 3 * n

    def body(*refs):
        lnd = refs[:n]
        send_sems, recv_sems = refs[n:n + n_sem], refs[n + n_sem:n + 2 * n_sem]
        x, y, c, chips = _place()
        for k in range(n):
            for j, (px, py) in enumerate(chips):
                pltpu.make_async_remote_copy(
                    src_ref=_half_rows(lnd[k], 2 * px + py, c), dst_ref=_half_rows(lnd[k], 2 * px + py, c),
                    send_sem=send_sems[3 * k + j], recv_sem=recv_sems[3 * k + j],
                    device_id=(x, y, 1 - c), device_id_type=MESH).start()
        refs[-1][...] = jnp.zeros_like(refs[-1])

    out = pl.pallas_call(
        body, name=name,
        out_shape=(*[pltpu.SemaphoreType.DMA(())] * (2 * n_sem), *[pltpu.HBM(l.shape, l.dtype) for l in lands],
                   jax.ShapeDtypeStruct((8, 128), F32)),
        in_specs=[HBM_SPEC] * n,
        out_specs=(*[SEM_SPEC] * (2 * n_sem), *[HBM_SPEC] * n, VMEM_SPEC),
        input_output_aliases={i: 2 * n_sem + i for i in range(n)},
        compiler_params=pltpu.CompilerParams(has_side_effects=EFFECT),
    )(*[_hbm(l) for l in lands])
    return list(out[:2 * n_sem]), list(out[2 * n_sem:2 * n_sem + n]), out[-1]


def _forward_wait(sems, lands, after, name):
    n = len(lands)
    n_sem = 3 * n

    def body(*refs):
        lnd = refs[:n]
        s_sems, r_sems = refs[n:n + n_sem], refs[n + n_sem:n + 2 * n_sem]
        x, y, c, chips = _place()
        for k in range(n):
            for j, (px, py) in enumerate(chips):
                cp = pltpu.make_async_remote_copy(
                    src_ref=_half_rows(lnd[k], 2 * px + py, c), dst_ref=_half_rows(lnd[k], 2 * px + py, 1 - c),
                    send_sem=s_sems[3 * k + j], recv_sem=r_sems[3 * k + j],
                    device_id=(x, y, 1 - c), device_id_type=MESH)
                cp.wait_send()
                cp.wait_recv()

    out = pl.pallas_call(
        body, name=name,
        out_shape=tuple(pltpu.HBM(l.shape, l.dtype) for l in lands),
        in_specs=[HBM_SPEC] * n + [SEM_SPEC] * (2 * n_sem) + [ANY_SPEC] * len(after),
        out_specs=[HBM_SPEC] * n,
        input_output_aliases={i: i for i in range(n)},
        compiler_params=pltpu.CompilerParams(has_side_effects=EFFECT),
    )(*lands, *sems, *after)
    return list(out)


def _blocks_finish(lands, name):
    n = len(lands)

    def body(*refs):
        lnd = refs[n:2 * n]
        send_sems, recv_sems = refs[2 * n:]
        x, y, c, chips = _place()
        sibling = (x, y, 1 - c)

        def copy(k, j, chip_id, pc):
            return pltpu.make_async_remote_copy(
                src_ref=_half_rows(lnd[k], chip_id, pc), dst_ref=_half_rows(lnd[k], chip_id, pc),
                send_sem=send_sems.at[k, j], recv_sem=recv_sems.at[k, j], device_id=sibling, device_id_type=MESH)

        started = []
        for k in range(n):
            for j, (px, py) in enumerate(chips):
                cp = copy(k, j, 2 * px + py, c)
                cp.start()
                started.append(cp)
        for k in range(n):
            for j, (px, py) in enumerate(chips):
                copy(k, j, 2 * px + py, 1 - c).wait_recv()
        for cp in started:
            cp.wait_send()

    out = pl.pallas_call(
        body, name=name,
        out_shape=[jax.ShapeDtypeStruct(l.shape, l.dtype) for l in lands],
        in_specs=[ANY_SPEC] * n, out_specs=[ANY_SPEC] * n,
        input_output_aliases={i: i for i in range(n)},
        scratch_shapes=[pltpu.SemaphoreType.DMA((n, 3)), pltpu.SemaphoreType.DMA((n, 3))],
    )(*lands)
    return list(out)


def _send_half_start(arrs, name, after=()):
    n = len(arrs)
    first = 2 * n + len(after)

    def body(*refs):
        ins, lnd = refs[:n], refs[n:2 * n]
        send_sems, recv_sems = refs[first:first + n], refs[first + n:first + 2 * n]
        token = refs[-1]
        x, y, c, _ = _place()
        for k in range(n):
            pltpu.make_async_remote_copy(
                src_ref=_half_of(ins[k], 1 - c), dst_ref=lnd[k], send_sem=send_sems[k], recv_sem=recv_sems[k],
                device_id=(x, y, 1 - c), device_id_type=MESH).start()
        token[...] = jnp.zeros_like(token)

    lands = [_hbm(lax.empty(_half_shape(a.shape), a.dtype)) for a in arrs]
    out = pl.pallas_call(
        body, name=name,
        out_shape=(*[pltpu.SemaphoreType.DMA(())] * (2 * n), *[pltpu.HBM(a.shape, a.dtype) for a in arrs],
                   *[pltpu.HBM(l.shape, l.dtype) for l in lands], jax.ShapeDtypeStruct((8, 128), F32)),
        in_specs=[HBM_SPEC] * (2 * n) + [ANY_SPEC] * len(after),
        out_specs=(*[SEM_SPEC] * (2 * n), *[HBM_SPEC] * (2 * n), VMEM_SPEC),
        input_output_aliases={i: 2 * n + i for i in range(2 * n)},
        compiler_params=pltpu.CompilerParams(has_side_effects=EFFECT),
    )(*[_hbm(a) for a in arrs], *lands, *after)
    return list(out[:2 * n]), list(out[2 * n:3 * n]), list(out[3 * n:4 * n]), out[-1]


def _send_half_wait(sems, arrs, lands, after, name):
    n = len(arrs)

    def body(*refs):
        ins, lnd = refs[:n], refs[n:2 * n]
        s_sems, r_sems = refs[2 * n:3 * n], refs[3 * n:4 * n]
        x, y, c, _ = _place()
        for k in range(n):
            cp = pltpu.make_async_remote_copy(
                src_ref=_half_of(ins[k], 1 - c), dst_ref=lnd[k], send_sem=s_sems[k], recv_sem=r_sems[k],
                device_id=(x, y, 1 - c), device_id_type=MESH)
            cp.wait_send()
            cp.wait_recv()

    out = pl.pallas_call(
        body, name=name,
        out_shape=tuple(pltpu.HBM(a.shape, a.dtype) for a in list(arrs) + list(lands)),
        in_specs=[HBM_SPEC] * (2 * n) + [SEM_SPEC] * (2 * n) + [ANY_SPEC] * len(after),
        out_specs=[HBM_SPEC] * (2 * n),
        input_output_aliases={i: i for i in range(2 * n)},
        compiler_params=pltpu.CompilerParams(has_side_effects=EFFECT),
    )(*arrs, *lands, *sems, *after)
    return list(out[:n]), list(out[n:])


def _scatter_start(arrs, name, after=()):
    n = len(arrs)
    n_sem = 3 * n
    first = 2 * n + len(after)

    def body(*refs):
        ins, lnd = refs[:n], refs[n:2 * n]
        send_sems, recv_sems = refs[first:first + n_sem], refs[first + n_sem:first + 2 * n_sem]
        token = refs[-1]
        x, y, c, chips = _place()
        me_chip = 2 * x + y
        for k in range(n):
            for j, (px, py) in enumerate(chips):
                pltpu.make_async_remote_copy(
                    src_ref=ins[k].at[2 * px + py], dst_ref=lnd[k].at[me_chip],
                    send_sem=send_sems[3 * k + j], recv_sem=recv_sems[3 * k + j],
                    device_id=(px, py, c), device_id_type=MESH).start()
        token[...] = jnp.zeros_like(token)

    lands = [_hbm(lax.empty(a.shape, a.dtype)) for a in arrs]
    out = pl.pallas_call(
        body, name=name,
        out_shape=(*[pltpu.SemaphoreType.DMA(())] * (2 * n_sem),
                   *[pltpu.HBM(a.shape, a.dtype) for a in arrs], *[pltpu.HBM(a.shape, a.dtype) for a in arrs],
                   jax.ShapeDtypeStruct((8, 128), F32)),
        in_specs=[HBM_SPEC] * (2 * n) + [ANY_SPEC] * len(after),
        out_specs=(*[SEM_SPEC] * (2 * n_sem), *[HBM_SPEC] * (2 * n), VMEM_SPEC),
        input_output_aliases={i: 2 * n_sem + i for i in range(2 * n)},
        compiler_params=pltpu.CompilerParams(has_side_effects=EFFECT),
    )(*[_hbm(a) for a in arrs], *lands, *after)
    base = 2 * n_sem
    return list(out[:base]), list(out[base:base + n]), list(out[base + n:base + 2 * n]), out[-1]


def _scatter_wait(sems, arrs, lands, after, name):
    n = len(arrs)
    n_sem = 3 * n

    def body(*refs):
        ins, lnd = refs[:n], refs[n:2 * n]
        s_sems, r_sems = refs[2 * n:2 * n + n_sem], refs[2 * n + n_sem:2 * n + 2 * n_sem]
        x, y, c, chips = _place()
        for k in range(n):
            for j, (px, py) in enumerate(chips):
                cp = pltpu.make_async_remote_copy(
                    src_ref=ins[k].at[2 * px + py], dst_ref=lnd[k].at[2 * px + py],
                    send_sem=s_sems[3 * k + j], recv_sem=r_sems[3 * k + j],
                    device_id=(px, py, c), device_id_type=MESH)
                cp.wait_send()
                cp.wait_recv()

    out = pl.pallas_call(
        body, name=name,
        out_shape=tuple(pltpu.HBM(a.shape, a.dtype) for a in list(arrs) + list(lands)),
        in_specs=[HBM_SPEC] * (2 * n) + [SEM_SPEC] * (2 * n_sem) + [ANY_SPEC] * len(after),
        out_specs=[HBM_SPEC] * (2 * n),
        input_output_aliases={i: i for i in range(2 * n)},
        compiler_params=pltpu.CompilerParams(has_side_effects=EFFECT),
    )(*arrs, *lands, *sems, *after)
    return list(out[:n]), list(out[n:])


def _sum_owner(chip_arr, pairs, got, name):
    nb, h, cols = got.shape
    tr = _row_tile(h, 16, 512)

    def body(chip_ref, own_ref, a_ref, b_ref, c_ref, o_ref):
        o_ref[...] = ((own_ref[0].astype(F32) + a_ref[0].astype(F32)) + b_ref[0].astype(F32)) + c_ref[0].astype(F32)

    def slot(off):
        return pl.BlockSpec((1, tr, cols), lambda i, chip_ref: ((chip_ref[0] + off) % N_CHIP, i, 0))

    return pl.pallas_call(
        body, name=name,
        grid_spec=pltpu.PrefetchScalarGridSpec(
            num_scalar_prefetch=1, grid=(h // tr,),
            in_specs=[slot(0), slot(1), slot(2), slot(3)],
            out_specs=pl.BlockSpec((tr, cols), lambda i, chip_ref: (i, 0))),
        out_shape=jax.ShapeDtypeStruct((h, cols), F32),
        compiler_params=_cparams(dimension_semantics=("parallel",)),
    )(chip_arr, pairs, got, got, got)


def _swap_start(arrs, name, after=()):
    n = len(arrs)
    first = 2 * n + len(after)

    def body(*refs):
        ins, lnd = refs[:n], refs[n:2 * n]
        send_sems, recv_sems = refs[first:first + n], refs[first + n:first + 2 * n]
        x, y, c, _ = _place()
        for k in range(n):
            pltpu.make_async_remote_copy(
                src_ref=ins[k], dst_ref=lnd[k], send_sem=send_sems[k], recv_sem=recv_sems[k],
                device_id=(x, y, 1 - c), device_id_type=MESH).start()

    lands = [_hbm(lax.empty(a.shape, a.dtype)) for a in arrs]
    out = pl.pallas_call(
        body, name=name,
        out_shape=(*[pltpu.SemaphoreType.DMA(())] * (2 * n), *[pltpu.HBM(a.shape, a.dtype) for a in arrs],
                   *[pltpu.HBM(a.shape, a.dtype) for a in arrs]),
        in_specs=[HBM_SPEC] * (2 * n) + [ANY_SPEC] * len(after),
        out_specs=(*[SEM_SPEC] * (2 * n), *[HBM_SPEC] * (2 * n)),
        input_output_aliases={i: 2 * n + i for i in range(2 * n)},
        compiler_params=pltpu.CompilerParams(has_side_effects=EFFECT),
    )(*[_hbm(a) for a in arrs], *lands, *after)
    return list(out[:2 * n]), list(out[2 * n:3 * n]), list(out[3 * n:4 * n])


def _swap_wait(sems, arrs, lands, after, name):
    n = len(arrs)

    def body(*refs):
        ins, lnd = refs[:n], refs[n:2 * n]
        s_sems, r_sems = refs[2 * n:3 * n], refs[3 * n:4 * n]
        x, y, c, _ = _place()
        for k in range(n):
            cp = pltpu.make_async_remote_copy(
                src_ref=ins[k], dst_ref=lnd[k], send_sem=s_sems[k], recv_sem=r_sems[k],
                device_id=(x, y, 1 - c), device_id_type=MESH)
            cp.wait_send()
            cp.wait_recv()

    out = pl.pallas_call(
        body, name=name,
        out_shape=tuple(pltpu.HBM(a.shape, a.dtype) for a in list(arrs) + list(lands)),
        in_specs=[HBM_SPEC] * (2 * n) + [SEM_SPEC] * (2 * n) + [ANY_SPEC] * len(after),
        out_specs=[HBM_SPEC] * (2 * n),
        input_output_aliases={i: i for i in range(2 * n)},
        compiler_params=pltpu.CompilerParams(has_side_effects=EFFECT),
    )(*arrs, *lands, *sems, *after)
    return list(out[:n]), list(out[n:])


def _row_tile(h, mult=8, cap=512):
    for t in range(cap - cap % mult, mult - 1, -mult):
        if h % t == 0:
            return t
    if mult > 8:
        return _row_tile(h, 8, cap)
    raise ValueError(h)


def _pair_sum(c_arr, full, recv, name):
    nb, rows, cols = full.shape

    def body(c_ref, f_ref, r_ref, o_ref):
        o_ref[...] = (f_ref[...] + r_ref[...]).astype(BF16)

    if _halved_by_rows(full.shape):
        h = rows // 2
        tr = _row_tile(h, 16, 512)
        steps = h // tr
        own = pl.BlockSpec((1, tr, cols), lambda b, i, c_ref: (b, c_ref[0] * steps + i, 0))
        half = pl.BlockSpec((1, tr, cols), lambda b, i, c_ref: (b, i, 0))
    else:
        steps = 1
        own = pl.BlockSpec((1, rows, cols // 2), lambda b, i, c_ref: (b, 0, c_ref[0]))
        half = pl.BlockSpec((1, rows, cols // 2), lambda b, i, c_ref: (b, 0, 0))
    return pl.pallas_call(
        body, name=name,
        grid_spec=pltpu.PrefetchScalarGridSpec(
            num_scalar_prefetch=1, grid=(nb, steps), in_specs=[own, half], out_specs=half),
        out_shape=jax.ShapeDtypeStruct(_half_shape(full.shape), BF16),
        compiler_params=_cparams(dimension_semantics=("parallel", "parallel")),
    )(c_arr, full, recv)


def _adam_math(g, w, m, v):
    m1 = ADAM_B1 * m + (1.0 - ADAM_B1) * g
    v1 = ADAM_B2 * v + (1.0 - ADAM_B2) * (g * g)
    m_hat = m1 / (1.0 - ADAM_B1 ** ADAM_STEP)
    v_hat = v1 / (1.0 - ADAM_B2 ** ADAM_STEP)
    delta = -ADAM_LR * (m_hat / (jnp.sqrt(v_hat) + ADAM_EPS) + ADAM_WD * w)
    return delta, m1, v1


def _adamw_halves(c_arr, own, other, w, m, v, name):
    rows, cols = w.shape
    by_rows = own.shape[1] == cols

    def body(c_ref, own_ref, oth_ref, w_ref, m_ref, v_ref, g_out, d_out, m_out, v_out):
        if by_rows:
            g = jnp.where(pl.program_id(0) == c_ref[0], own_ref[...], oth_ref[...])
        else:
            own_, oth_ = own_ref[...], oth_ref[...]
            g = jnp.where(c_ref[0] == 0, jnp.concatenate([own_, oth_], axis=1), jnp.concatenate([oth_, own_], axis=1))
        d, m1, v1 = _adam_math(g, w_ref[...], m_ref[...], v_ref[...])
        g_out[...] = g
        d_out[...] = d
        m_out[...] = m1
        v_out[...] = v1

    if by_rows:
        h = rows // 2
        tr = _row_tile(h)
        steps = h // tr
        grid = (2, steps)
        half_spec = pl.BlockSpec((tr, cols), lambda p, i, c_ref: (i, 0))
        full_spec = pl.BlockSpec((tr, cols), lambda p, i, c_ref: (p * steps + i, 0))
    else:
        tr = _row_tile(rows)
        grid = (1, rows // tr)
        half_spec = pl.BlockSpec((tr, cols // 2), lambda p, i, c_ref: (i, 0))
        full_spec = pl.BlockSpec((tr, cols), lambda p, i, c_ref: (i, 0))
    return pl.pallas_call(
        body, name=name,
        grid_spec=pltpu.PrefetchScalarGridSpec(
            num_scalar_prefetch=1, grid=grid,
            in_specs=[half_spec, half_spec, full_spec, full_spec, full_spec],
            out_specs=[full_spec] * 4),
        out_shape=[jax.ShapeDtypeStruct(w.shape, F32)] * 4,
        compiler_params=_cparams(dimension_semantics=("parallel", "parallel")),
    )(c_arr, own, other, w, m, v)


def _adamw_whole(items, name):
    n = len(items)

    def body(*refs):
        ins, outs = refs[:4 * n], refs[4 * n:]
        for k in range(n):
            g, w, m, v = (r[...] for r in ins[4 * k:4 * k + 4])
            d, m1, v1 = _adam_math(g, w, m, v)
            outs[3 * k][...] = d
            outs[3 * k + 1][...] = m1
            outs[3 * k + 2][...] = v1

    flat = [a for it in items for a in it]
    shapes = [jax.ShapeDtypeStruct(it[1].shape, F32) for it in items for _ in range(3)]
    out = pl.pallas_call(
        body, name=name, out_shape=shapes,
        in_specs=[VMEM_SPEC] * (4 * n), out_specs=[VMEM_SPEC] * (3 * n),
        compiler_params=_cparams(),
    )(*flat)
    return [tuple(out[3 * k:3 * k + 3]) for k in range(n)]


def _adamw_tiled(g, w, m, v, name):
    rows, cols = w.shape
    tr = _row_tile(rows)

    def body(g_ref, w_ref, m_ref, v_ref, d_out, m_out, v_out):
        d, m1, v1 = _adam_math(g_ref[...], w_ref[...], m_ref[...], v_ref[...])
        d_out[...] = d
        m_out[...] = m1
        v_out[...] = v1

    spec = pl.BlockSpec((tr, cols), lambda i: (i, 0))
    return pl.pallas_call(
        body, name=name, grid=(rows // tr,),
        out_shape=[jax.ShapeDtypeStruct(w.shape, F32)] * 3,
        in_specs=[spec] * 4, out_specs=[spec] * 3,
        compiler_params=_cparams(dimension_semantics=("parallel",)),
    )(g, w, m, v)


def _mod_forward(cond, w_mod, b_mod_cols, name):
    def body(c_ref, w_ref, b_ref, o_ref):
        o_ref[...] = _dot(_silu(c_ref[...]), w_ref[...]) + b_ref[...]

    return pl.pallas_call(
        body, name=name, out_shape=jax.ShapeDtypeStruct((cond.shape[0], w_mod.shape[1]), F32),
        in_specs=[VMEM_SPEC] * 3, out_specs=VMEM_SPEC, compiler_params=_cparams(),
    )(cond, w_mod, b_mod_cols)


def _mod_backward(cond, w_mod, dmod_cols, name):
    def body(c_ref, w_ref, d_ref, gw_ref, gc_ref):
        s = _silu(c_ref[...])
        d = d_ref[...]
        gw_ref[...] = _dot_tn(s, d)
        gc_ref[...] = _dot_nt(d[8:16, :], w_ref[...])

    return pl.pallas_call(
        body, name=name,
        out_shape=[jax.ShapeDtypeStruct(w_mod.shape, F32), jax.ShapeDtypeStruct((8, w_mod.shape[0]), F32)],
        in_specs=[VMEM_SPEC] * 3, out_specs=[VMEM_SPEC] * 2, compiler_params=_cparams(),
    )(cond, w_mod, dmod_cols)


def _col_chunks(width, step=512):
    return [(s, min(step, width - s)) for s in range(0, width, step)]


def _w_in_row(p_off):
    if p_off < 9 * HW:
        return p_off
    return 9 * HW if p_off == OFF_LR else p_off + 2 * RANK


def _in_projection(ctx0, x0, modc, modx, pre1, w_t, n_ctx_tiles, name):
    d = x0.shape[1]
    rows = ctx0.shape[0] + x0.shape[0]
    width = P_WIDTH

    pieces = _col_chunks(w_t.shape[0])

    def body(ctx_ref, x_ref, modc_ref, modx_ref, pre_ref, w_hbm, h_ref, p_ref, w_ref, sems):
        first = pl.program_id(0) == 0

        def piece_copy(k):
            r0, n_rows = pieces[k]
            return pltpu.make_async_copy(w_hbm.at[r0:r0 + n_rows, :], w_ref.at[r0:r0 + n_rows, :], sems.at[k])

        @pl.when(first)
        def _():
            for k in range(len(pieces)):
                piece_copy(k).start()

        is_ctx = pl.program_id(0) < n_ctx_tiles
        n, _ = _rms(jnp.where(is_ctx, ctx_ref[...], x_ref[...]))
        shift = jnp.where(is_ctx, modc_ref[0:1, :], modx_ref[0:1, :])
        scale = jnp.where(is_ctx, modc_ref[1:2, :], modx_ref[1:2, :])
        h = (n * pre_ref[...] * (1.0 + scale) + shift).astype(BF16)
        h_ref[...] = h
        waited = set()
        for s, w in _col_chunks(width):
            r0 = _w_in_row(s)
            needed = [k for k, (q0, nq) in enumerate(pieces) if q0 < r0 + w and r0 < q0 + nq and k not in waited]
            waited.update(needed)

            @pl.when(first)
            def _(needed=needed):
                for k in needed:
                    piece_copy(k).wait()

            p_ref[:, s:s + w] = _dot_nt(h, w_ref[r0:r0 + w, :])
        assert len(waited) == len(pieces)

    row = lambda i: (i, 0)
    fixed = lambda i: (0, 0)
    return pl.pallas_call(
        body, name=name, grid=(rows // TM,),
        out_shape=[jax.ShapeDtypeStruct((rows, d), BF16), jax.ShapeDtypeStruct((rows, width), F32)],
        in_specs=[pl.BlockSpec((TM, d), lambda i: (jnp.minimum(i, n_ctx_tiles - 1), 0)),
                  pl.BlockSpec((TM, d), lambda i: (jnp.maximum(i - n_ctx_tiles, 0), 0)),
                  pl.BlockSpec((8, d), fixed), pl.BlockSpec((8, d), fixed), pl.BlockSpec((1, d), fixed), ANY_SPEC],
        out_specs=[pl.BlockSpec((TM, d), row), pl.BlockSpec((TM, width), row)],
        scratch_shapes=[pltpu.VMEM(w_t.shape, w_t.dtype), pltpu.SemaphoreType.DMA((len(pieces),))],
        compiler_params=_cparams(dimension_semantics=("arbitrary",)),
    )(ctx0, x0, modc, modx, pre1, w_t)


C_HQ, C_HI, C_HF_FW, C_HF_BW, C_HGATE, C_GQ, C_GK, C_GV, C_GGATE = range(9)
OFF_GATE_HG = 9 * HW
OFF_LR = 13 * HW
P_WIDTH = OFF_LR + 128


def _head_norm_fwd(o, w):
    outs, ns, rs = [], [], []
    for h in range(NH):
        n, r = _rms(o[:, h * HD:(h + 1) * HD])
        ns.append(n)
        rs.append(r)
        outs.append(n * w)
    return jnp.concatenate(outs, axis=1), ns, rs


def _mixer_tail(z, o_hg, o_gla, p_hgate, p_ggate, p_gate_hg, p_gate_gla, hg_on, gla_on, wbh, wbg, wout):
    on_hg, n_hg, r_hg = _head_norm_fwd(o_hg, hg_on)
    on_gla, n_gla, r_gla = _head_norm_fwd(o_gla, gla_on)
    og_hg = (on_hg * _silu(p_hgate)).astype(BF16)
    og_gla = (on_gla * _silu(p_ggate)).astype(BF16)
    b_hg = jnp.dot(og_hg, wbh, preferred_element_type=F32)
    b_gla = jnp.dot(og_gla, wbg, preferred_element_type=F32)
    s_hg = _sigmoid(p_gate_hg)
    s_gla = _sigmoid(p_gate_gla)
    merged = (s_hg * b_hg + s_gla * b_gla).astype(BF16)
    y1 = jnp.dot(merged, wout, preferred_element_type=F32)
    return dict(on_hg=on_hg, n_hg=n_hg, r_hg=r_hg, on_gla=on_gla, n_gla=n_gla, r_gla=r_gla, og_hg=og_hg,
                og_gla=og_gla, b_hg=b_hg, b_gla=b_gla, s_hg=s_hg, s_gla=s_gla, merged=merged, y1=y1)


def _mixer_tail_fwd(x_lat, p, o_list, modx, norms, onorms, w_br_hg, w_br_gla, w_out, n_ctx_tiles, name):
    rows, d = x_lat.shape

    def body(x_ref, ofw_hg, obw_hg, ofw_gla, obw_gla, p_hgate, p_ggate, p_ghg_a, p_ghg_b, p_ggla_a, p_ggla_b,
             modx_ref, norm_ref, on_ref, wbh_ref, wbg_ref, wout_ref, z2_ref, y1_ref, mrg_ref, oghg_ref, oggla_ref):
        p_gate_hg = jnp.concatenate([p_ghg_a[...], p_ghg_b[...]], axis=1)
        p_gate_gla = jnp.concatenate([p_ggla_a[...], p_ggla_b[...]], axis=1)
        t = _mixer_tail(x_ref[...], ofw_hg[...] + obw_hg[...], ofw_gla[...] + obw_gla[...], p_hgate[...],
                        p_ggate[...], p_gate_hg, p_gate_gla, on_ref[0:1, 0:HD], on_ref[1:2, 0:HD],
                        wbh_ref[...], wbg_ref[...], wout_ref[...])
        y1_ref[...] = t["y1"]
        mrg_ref[...] = t["merged"]
        oghg_ref[...] = t["og_hg"]
        oggla_ref[...] = t["og_gla"]
        n1, _ = _rms(t["y1"])
        z2_ref[...] = x_ref[...] + n1 * norm_ref[1:2, :] * modx_ref[2:3, :]

    lat = lambda i: (i, 0)
    full = lambda i: (i + n_ctx_tiles, 0)
    fixed = lambda i: (0, 0)

    def pcol(blk):
        return pl.BlockSpec((TM, HW), lambda i: (i + n_ctx_tiles, blk))

    in_specs = ([pl.BlockSpec((TM, d), lat)] + [pl.BlockSpec((TM, HW), full)] * 4
                + [pcol(C_HGATE), pcol(C_GGATE), pcol(9), pcol(10), pcol(11), pcol(12)]
                + [pl.BlockSpec((8, d), fixed)] * 3 + [VMEM_SPEC] * 3)
    bf = lambda w: jax.ShapeDtypeStruct((rows, w), BF16)
    f32 = jax.ShapeDtypeStruct((rows, d), F32)
    return pl.pallas_call(
        body, name=name, grid=(rows // TM,), out_shape=[f32, f32, bf(d), bf(HW), bf(HW)], in_specs=in_specs,
        out_specs=[pl.BlockSpec((TM, d), lat)] * 3 + [pl.BlockSpec((TM, HW), lat)] * 2,
        compiler_params=_cparams(dimension_semantics=("parallel",)),
    )(x_lat, *o_list, p, p, p, p, p, p, modx, norms, onorms, w_br_hg, w_br_gla, w_out)


def _ffn_fwd_bwd(z2, modx, norms, w_gate, w_up, w_down, target, name):
    rows, d = z2.shape
    dff = w_gate.shape[0]
    inv_d = 1.0 / d

    def body(z2_ref, modx_ref, norm_ref, wg_ref, wu_ref, wd_ref, t_ref,
             loss_ref, dz2_ref, h2_ref, a_ref, du_ref, dv_ref, dy2_ref, stat_ref):
        i = pl.program_id(0)
        pre2, post2 = norm_ref[2:3, :], norm_ref[3:4, :]
        shift2, scale2, gate2 = modx_ref[3:4, :], modx_ref[4:5, :], modx_ref[5:6, :]
        z2 = z2_ref[...]
        n2, r2 = _rms(z2)
        nw2 = n2 * pre2
        h2 = (nw2 * (1.0 + scale2) + shift2).astype(BF16)
        h2_ref[...] = h2
        u = _dot_nt(h2, wg_ref[...])
        v = _dot_nt(h2, wu_ref[...])
        su = _silu(u)
        a = (su * v).astype(BF16)
        a_ref[...] = a
        y2 = jnp.dot(a, wd_ref[...], preferred_element_type=F32)
        n3, r3 = _rms(y2)
        err = z2 + n3 * post2 * gate2 - t_ref[...]
        part = 0.5 * inv_d * jnp.sum(err * err)
        dz3 = err * inv_d
        dgate2 = _colsum(dz3 * n3 * post2)
        tt = dz3 * gate2
        dpost2 = _colsum(tt * n3)
        dy2 = _rms_bwd(tt * post2, n3, r3).astype(BF16)
        dy2_ref[...] = dy2
        da = _dot_nt(dy2, wd_ref[...])
        du = (da * v * _dsilu(u)).astype(BF16)
        dv = (da * su).astype(BF16)
        du_ref[...] = du
        dv_ref[...] = dv
        dh2 = (jnp.dot(du, wg_ref[...], preferred_element_type=F32)
               + jnp.dot(dv, wu_ref[...], preferred_element_type=F32))
        dshift2 = _colsum(dh2)
        dscale2 = _colsum(dh2 * nw2)
        dnw2 = dh2 * (1.0 + scale2)
        dpre2 = _colsum(dnw2 * n2)
        dz2_ref[...] = dz3 + _rms_bwd(dnw2 * pre2, n2, r2)

        @pl.when(i == 0)
        def _():
            stat_ref[...] = jnp.zeros_like(stat_ref)
            loss_ref[...] = jnp.zeros_like(loss_ref)

        for r, val in enumerate((dshift2, dscale2, dgate2, dpre2, dpost2)):
            stat_ref[r:r + 1, :] += val
        loss_ref[...] += part
        stat_ref[5:6, 0:128] += part

    lat = lambda i: (i, 0)
    fixed = lambda i: (0, 0)
    bf = lambda w: jax.ShapeDtypeStruct((rows, w), BF16)
    return pl.pallas_call(
        body, name=name, grid=(rows // TM,),
        out_shape=[jax.ShapeDtypeStruct((8, 128), F32), jax.ShapeDtypeStruct((rows, d), F32), bf(d), bf(dff), bf(dff),
                   bf(dff), bf(d), jax.ShapeDtypeStruct((8, d), F32)],
        in_specs=[pl.BlockSpec((TM, d), lat), pl.BlockSpec((8, d), fixed), pl.BlockSpec((8, d), fixed)]
        + [VMEM_SPEC] * 3 + [pl.BlockSpec((TM, d), lat)],
        out_specs=[pl.BlockSpec((8, 128), fixed), pl.BlockSpec((TM, d), lat), pl.BlockSpec((TM, d), lat),
                   pl.BlockSpec((TM, dff), lat), pl.BlockSpec((TM, dff), lat), pl.BlockSpec((TM, dff), lat),
                   pl.BlockSpec((TM, d), lat), pl.BlockSpec((8, d), fixed)],
        compiler_params=_cparams(dimension_semantics=("arbitrary",)),
    )(z2, modx, norms, w_gate, w_up, w_down, target)


def _mixer_tail_bwd(x_lat, p, o_list, dz2, y1, modx, norms, onorms, w_br_hg, w_br_gla, w_out, n_ctx_tiles, n_tiles,
                    name):
    rows, d = x_lat.shape
    total = n_tiles * TM

    def body(x_ref, ofw_hg, obw_hg, ofw_gla, obw_gla, p_hgate, p_ggate, p_ghg_a, p_ghg_b, p_ggla_a, p_ggla_b,
             dz2_ref, y1_ref, modx_ref, norm_ref, on_ref, wbh_ref, wbg_ref, wout_ref,
             dohg_ref, dogla_ref, dhgate_ref, dggate_ref, dghg_ref, dggla_ref, dy1_ref, dbhg_ref, dbgla_ref,
             stat_ref):
        i = pl.program_id(0)

        @pl.when(i == 0)
        def _():
            stat_ref[...] = jnp.zeros_like(stat_ref)

        @pl.when(i < n_ctx_tiles)
        def _():
            for ref in (dohg_ref, dogla_ref, dhgate_ref, dggate_ref, dghg_ref, dggla_ref):
                ref[...] = jnp.zeros_like(ref)

        @pl.when(i >= n_ctx_tiles)
        def _():
            post1, gate1 = norm_ref[1:2, :], modx_ref[2:3, :]
            hg_on, gla_on = on_ref[0:1, 0:HD], on_ref[1:2, 0:HD]
            p_gate_hg = jnp.concatenate([p_ghg_a[...], p_ghg_b[...]], axis=1)
            p_gate_gla = jnp.concatenate([p_ggla_a[...], p_ggla_b[...]], axis=1)
            ph, pg = p_hgate[...], p_ggate[...]
            t = _mixer_tail(x_ref[...], ofw_hg[...] + obw_hg[...], ofw_gla[...] + obw_gla[...], ph, pg,
                            p_gate_hg, p_gate_gla, hg_on, gla_on, wbh_ref[...], wbg_ref[...], wout_ref[...])
            dz2 = dz2_ref[...]
            n1, r1 = _rms(y1_ref[...])
            dgate1 = _colsum(dz2 * n1 * post1)
            tt = dz2 * gate1
            dpost1 = _colsum(tt * n1)
            dy1 = _rms_bwd(tt * post1, n1, r1).astype(BF16)
            dy1_ref[...] = dy1
            dmerged = _dot_nt(dy1, wout_ref[...])
            dghg_ref[...] = (dmerged * t["b_hg"] * t["s_hg"] * (1.0 - t["s_hg"])).astype(BF16)
            dggla_ref[...] = (dmerged * t["b_gla"] * t["s_gla"] * (1.0 - t["s_gla"])).astype(BF16)
            db_hg = (dmerged * t["s_hg"]).astype(BF16)
            db_gla = (dmerged * t["s_gla"]).astype(BF16)
            dbhg_ref[...] = db_hg
            dbgla_ref[...] = db_gla
            don_acc = []
            for (db, wb, pgate, on, ns, rs, gain, gate_ref, do_ref) in (
                    (db_hg, wbh_ref, ph, t["on_hg"], t["n_hg"], t["r_hg"], hg_on, dhgate_ref, dohg_ref),
                    (db_gla, wbg_ref, pg, t["on_gla"], t["n_gla"], t["r_gla"], gla_on, dggate_ref, dogla_ref)):
                dog = _dot_nt(db, wb[...])
                gate_ref[...] = (dog * on * _dsilu(pgate)).astype(BF16)
                don = dog * _silu(pgate)
                acc = jnp.zeros((1, HD), F32)
                for h in range(NH):
                    sl = slice(h * HD, (h + 1) * HD)
                    acc = acc + _colsum(don[:, sl] * ns[h])
                    do_ref[:, sl] = _rms_bwd(don[:, sl] * gain, ns[h], rs[h]).astype(BF16)
                don_acc.append(acc)
            stat_ref[0:1, :] += dgate1
            stat_ref[1:2, :] += dpost1
            stat_ref[2:3, 0:HD] += don_acc[0]
            stat_ref[2:3, HD:2 * HD] += don_acc[1]

    lat = lambda i: (jnp.maximum(i - n_ctx_tiles, 0), 0)
    full = lambda i: (i, 0)
    fixed = lambda i: (0, 0)

    def pcol(blk):
        return pl.BlockSpec((TM, HW), lambda i: (i, blk))

    in_specs = ([pl.BlockSpec((TM, d), lat)] + [pl.BlockSpec((TM, HW), full)] * 4
                + [pcol(C_HGATE), pcol(C_GGATE), pcol(9), pcol(10), pcol(11), pcol(12)]
                + [pl.BlockSpec((TM, d), lat), pl.BlockSpec((TM, d), lat)]
                + [pl.BlockSpec((8, d), fixed)] * 3 + [VMEM_SPEC] * 3)
    f = lambda w: jax.ShapeDtypeStruct((total, w), BF16)
    out_shape = [f(HW), f(HW), f(HW), f(HW), f(d), f(d), jax.ShapeDtypeStruct((rows, d), BF16),
                 jax.ShapeDtypeStruct((rows, d), BF16), jax.ShapeDtypeStruct((rows, d), BF16),
                 jax.ShapeDtypeStruct((8, d), F32)]
    out_specs = ([pl.BlockSpec((TM, HW), full)] * 4 + [pl.BlockSpec((TM, d), full)] * 2
                 + [pl.BlockSpec((TM, d), lat)] * 3 + [pl.BlockSpec((8, d), fixed)])
    return pl.pallas_call(
        body, name=name, grid=(n_tiles,), out_shape=out_shape, in_specs=in_specs, out_specs=out_specs,
        compiler_params=_cparams(dimension_semantics=("arbitrary",)),
    )(x_lat, *o_list, p, p, p, p, p, p, dz2, y1, modx, norms, onorms, w_br_hg, w_br_gla, w_out)


def _in_projection_bwd(ctx0, x0, dz2, modc, modx, pre1, w_t, pieces, n_ctx_tiles, name):
    d = x0.shape[1]
    rows = ctx0.shape[0] + x0.shape[0]
    lat_rows = dz2.shape[0]
    width = P_WIDTH
    n_pieces = len(pieces)

    def body(*refs):
        ctx_ref, x_ref, dz2_ref, modc_ref, modx_ref, pre_ref, w_ref = refs[:7]
        (dhq_f, dhq_b, dhi_f, dhi_b, dhf_f, dhf_b, dhgate, dgq_f, dgq_b, dgk_f, dgk_b, dgv_f, dgv_b, dggate,
         dghg, dggla, dlr_f, dlr_b) = refs[7:7 + n_pieces]
        dp_ref, gx_ref, stat_ref = refs[7 + n_pieces:]
        i = pl.program_id(0)
        is_ctx = i < n_ctx_tiles
        z = jnp.where(is_ctx, ctx_ref[...], x_ref[...])
        sections = [
            (0, dhq_f[...] + dhq_b[...]), (HW, dhi_f[...] + dhi_b[...]), (2 * HW, dhf_f[...]), (3 * HW, dhf_b[...]),
            (4 * HW, dhgate[...]), (5 * HW, dgq_f[...] + dgq_b[...]), (6 * HW, dgk_f[...] + dgk_b[...]),
            (7 * HW, dgv_f[...] + dgv_b[...]), (8 * HW, dggate[...]),
            (9 * HW, dghg[:, 0:HW]), (10 * HW, dghg[:, HW:2 * HW]),
            (11 * HW, dggla[:, 0:HW]), (12 * HW, dggla[:, HW:2 * HW]), (OFF_LR, dlr_f[...] + dlr_b[...])]
        dh = jnp.zeros((TM, d), F32)
        for off, val in sections:
            w = val.shape[1]
            vb = val.astype(BF16)
            dp_ref[off:off + w, :] = vb.T
            dh = dh + jnp.dot(vb, w_ref[_w_in_row(off):_w_in_row(off) + w, :], preferred_element_type=F32)
        n, r = _rms(z)
        pre = pre_ref[...]
        scale = jnp.where(is_ctx, modc_ref[1:2, :], modx_ref[1:2, :])
        nw = n * pre
        dshift = _colsum(dh)
        dscale = _colsum(dh * nw)
        dnw = dh * (1.0 + scale)
        dpre = _colsum(dnw * n)
        gx_ref[...] = dz2_ref[...] + _rms_bwd(dnw * pre, n, r)
        zero = jnp.zeros((1, d), F32)

        @pl.when(i == 0)
        def _():
            stat_ref[...] = jnp.zeros_like(stat_ref)

        stat_ref[0:1, :] += jnp.where(is_ctx, zero, dshift)
        stat_ref[1:2, :] += jnp.where(is_ctx, zero, dscale)
        stat_ref[2:3, :] += jnp.where(is_ctx, dshift, zero)
        stat_ref[3:4, :] += jnp.where(is_ctx, dscale, zero)
        stat_ref[4:5, :] += dpre

    full = lambda i: (i, 0)
    lat = lambda i: (jnp.maximum(i - n_ctx_tiles, 0), 0)
    fixed = lambda i: (0, 0)
    piece_specs = [pl.BlockSpec((TM, a.shape[1]), full) for a in pieces]
    in_specs = [pl.BlockSpec((TM, d), lambda i: (jnp.minimum(i, n_ctx_tiles - 1), 0)), pl.BlockSpec((TM, d), lat),
                pl.BlockSpec((TM, d), lat), pl.BlockSpec((8, d), fixed),
                pl.BlockSpec((8, d), fixed), pl.BlockSpec((1, d), fixed), VMEM_SPEC] + piece_specs
    return pl.pallas_call(
        body, name=name, grid=(rows // TM,),
        out_shape=[jax.ShapeDtypeStruct((width, rows), BF16), jax.ShapeDtypeStruct((lat_rows, d), F32),
                   jax.ShapeDtypeStruct((8, d), F32)],
        in_specs=in_specs,
        out_specs=[pl.BlockSpec((width, TM), lambda i: (0, i)), pl.BlockSpec((TM, d), lat),
                   pl.BlockSpec((8, d), fixed)],
        compiler_params=_cparams(dimension_semantics=("arbitrary",)),
    )(ctx0, x0, dz2, modc, modx, pre1, w_t, *pieces)


def _transposed_lhs_matmul(x_ref, dy_ref, o_ref, xt_ref):
    @pl.when(pl.program_id(1) == 0)
    def _():
        xt_ref[...] = x_ref[...].T

    o_ref[...] = jnp.dot(xt_ref[...], dy_ref[...], preferred_element_type=F32)


def _w_in_grad(dp_t, h1, n_cols, name, after=()):
    rows, d = h1.shape
    n_main = OFF_LR // HW
    lr0 = _w_in_row(OFF_LR)

    def body(x_ref, xlr_ref, h_ref, *rest):
        o_hbm, acc_ref, sems = rest[len(after):]
        i = pl.program_id(0)
        slot = i % 2

        def main_copy(step):
            row = jnp.where(step < 9, step * HW, step * HW + 2 * RANK)
            return pltpu.make_async_copy(acc_ref.at[step % 2], o_hbm.at[pl.ds(pl.multiple_of(row, 8), HW), :],
                                         sems.at[step % 2])

        @pl.when(i > 1)
        def _():
            main_copy(i - 2).wait()

        @pl.when(i < n_main)
        def _():
            acc_ref[slot] = jnp.dot(x_ref[...], h_ref[...], preferred_element_type=F32)
            main_copy(i).start()

        @pl.when(i == n_main)
        def _():
            acc_ref[slot, 0:128, :] = jnp.dot(xlr_ref[...], h_ref[...], preferred_element_type=F32)
            lr_copy = pltpu.make_async_copy(acc_ref.at[slot, 0:2 * RANK, :], o_hbm.at[lr0:lr0 + 2 * RANK, :],
                                            sems.at[slot])
            lr_copy.start()
            main_copy(i - 1).wait()
            lr_copy.wait()

    return pl.pallas_call(
        body, name=name, grid=(n_main + 1,),
        out_shape=jax.ShapeDtypeStruct((n_cols, d), F32),
        in_specs=[pl.BlockSpec((HW, rows), lambda i: (jnp.minimum(i, n_main - 1), 0)),
                  pl.BlockSpec((128, rows), lambda i: (OFF_LR // 128, 0)),
                  pl.BlockSpec((rows, d), lambda i: (0, 0))] + [ANY_SPEC] * len(after),
        out_specs=ANY_SPEC,
        scratch_shapes=[pltpu.VMEM((2, HW, d), F32), pltpu.SemaphoreType.DMA((2,))],
        compiler_params=_cparams(dimension_semantics=("arbitrary",)),
    )(dp_t, dp_t, h1, *after)


def _weight_grad(xs, dy, name, tk=None, tn=512, k_first=0, k_tiles=None):
    rows = dy.shape[0]
    n = dy.shape[1]
    tn_ = min(tn, n)
    tk_ = xs.shape[1] if tk is None else tk
    k_tiles = xs.shape[1] // tk_ if k_tiles is None else k_tiles
    k = k_tiles * tk_

    return pl.pallas_call(
        functools.partial(_transposed_lhs_matmul), name=name, grid=(k_tiles, n // tn_),
        out_shape=jax.ShapeDtypeStruct((k, n), F32),
        in_specs=[pl.BlockSpec((rows, tk_), lambda i, j: (0, i + k_first)),
                  pl.BlockSpec((rows, tn_), lambda i, j: (0, j))],
        out_specs=pl.BlockSpec((tk_, tn_), lambda i, j: (i, j)),
        scratch_shapes=[pltpu.VMEM((tk_, rows), BF16)],
        compiler_params=_cparams(dimension_semantics=("parallel", "arbitrary")),
    )(xs, dy)


def _running_sums(xs, fws):
    c = xs[0].shape[0]
    row = lax.broadcasted_iota(jnp.int32, (c, 1), 0)
    s = 1
    while s < c:
        xs = [x + (jnp.where(row >= s, pltpu.roll(x, s, axis=0), 0.0) if fw else
                   jnp.where(row < c - s, pltpu.roll(x, c - s, axis=0), 0.0)) for x, fw in zip(xs, fws)]
        s *= 2
    return xs


def _chunks_terms(qs, ks, gs, fws):
    c = CHUNK
    n = len(qs)
    r = lax.broadcasted_iota(jnp.int32, (c, c), 0)
    s = lax.broadcasted_iota(jnp.int32, (c, c), 1)
    row = lax.broadcasted_iota(jnp.int32, (c, 1), 0)
    per_dir = {}
    for fw in set(fws):
        pos = row if fw else (c - 1 - row)
        per_dir[fw] = dict(
            causal=(s <= r) if fw else (s >= r), causal_t=(s >= r) if fw else (s <= r), pos=pos,
            in_blk=[(pos >= SUB * j) & (pos < SUB * (j + 1)) for j in range(NSUB)],
            start_row=[None] + [SUB * j - 1 if fw else c - SUB * j for j in range(1, NSUB)],
            rend=c - 1 if fw else 0)
    dirs = [per_dir[fw] for fw in fws]
    cums = _running_sums(gs, fws)
    starts = [[None] + [cum[d["start_row"][j]:d["start_row"][j] + 1, :] for j in range(1, NSUB)]
              for cum, d in zip(cums, dirs)]
    es = [[jnp.exp(cum) for cum in cums]]
    for j in range(1, NSUB):
        es.append([jnp.exp(jnp.where(d["pos"] >= SUB * j, cum - st[j], -1e30)) for cum, st, d in zip(cums, starts, dirs)])
    owns = [functools.reduce(lambda rest, j: jnp.where(d["in_blk"][j], st[j], rest), range(1, NSUB), 0.0)
            for st, d in zip(starts, dirs)]
    kscales = [jnp.exp(own - cum) for own, cum in zip(owns, cums)]
    cends = [cum[d["rend"]:d["rend"] + 1, :] for cum, d in zip(cums, dirs)]
    tails = [jnp.exp(cend - cum) for cend, cum in zip(cends, cums)]
    qcats = [jnp.concatenate([q * es[j][i] for j in range(NSUB)], axis=1).astype(BF16) for i, q in enumerate(qs)]
    kts = [k * ksc for k, ksc in zip(ks, kscales)]
    kms = [jnp.concatenate([jnp.where(d["in_blk"][j], kt, 0.0) for j in range(NSUB)], axis=1).astype(BF16)
           for kt, d in zip(kts, dirs)]
    e_by_lane = [[es[j][i] for j in range(NSUB)] for i in range(n)]
    return dict(dirs=dirs, e=e_by_lane, kscale=kscales, cend=cends, tail=tails, qcat=qcats, km=kms, kt=kts)


def _chunks_fwd(qs, ks, vs, gs, st0s, fws):
    t = _chunks_terms(qs, ks, gs, fws)
    scores = [_dot_nt(qc, km) for qc, km in zip(t["qcat"], t["km"])]
    a = [jnp.where(d["causal"], sc, 0.0) for sc, d in zip(scores, t["dirs"])]
    inter = [_dot_nt(qc[:, 0:HD], st0) for qc, st0 in zip(t["qcat"], st0s)]
    intra = [_dot(a_, v) for a_, v in zip(a, vs)]
    os_ = [x + y for x, y in zip(intra, inter)]
    upd = [_dot_tn(v, k * tl) for v, k, tl in zip(vs, ks, t["tail"])]
    st1s = [st0 * jnp.exp(ce) + u for st0, ce, u in zip(st0s, t["cend"], upd)]
    return os_, st1s


def _chunks_bwd(qs, ks, vs, gs, st0s, dos, dst1s, fws):
    n = len(qs)
    t = _chunks_terms(qs, ks, gs, fws)
    qcat, km, e, dirs = t["qcat"], t["km"], t["e"], t["dirs"]
    a_t = [jnp.where(d["causal_t"], _dot_nt(km_, qc), 0.0) for km_, qc, d in zip(km, qcat, dirs)]
    ktail = [k * tl for k, tl in zip(ks, t["tail"])]
    dv_a = [_dot(at, do) for at, do in zip(a_t, dos)]
    dv_b = [_dot_nt(kt, ds) for kt, ds in zip(ktail, dst1s)]
    dv = [x + y for x, y in zip(dv_a, dv_b)]
    da = [jnp.where(d["causal"], _dot_nt(do, v), 0.0) for do, v, d in zip(dos, vs, dirs)]
    da_t = [jnp.where(d["causal_t"], _dot_nt(v, do), 0.0) for do, v, d in zip(dos, vs, dirs)]
    dqcat = [_dot(da_, km_) for da_, km_ in zip(da, km)]
    dq_inter = [e[i][0] * _dot(dos[i], st0s[i]) for i in range(n)]
    dkm = [_dot(dat, qc) for dat, qc in zip(da_t, qcat)]
    dk_inter = [_dot(v, ds) * tl for v, ds, tl in zip(vs, dst1s, t["tail"])]
    dq = [dq_inter[i] + sum(e[i][j] * dqcat[i][:, j * HD:(j + 1) * HD] for j in range(NSUB)) for i in range(n)]
    dkt = [sum(jnp.where(dirs[i]["in_blk"][j], dkm[i][:, j * HD:(j + 1) * HD], 0.0) for j in range(NSUB))
           for i in range(n)]
    dk = [dkt[i] * t["kscale"][i] + dk_inter[i] for i in range(n)]
    dcum = [qs[i] * dq_inter[i] - ks[i] * dk_inter[i] - t["kt"][i].astype(BF16).astype(F32) * dkt[i]
            + sum(qcat[i][:, j * HD:(j + 1) * HD].astype(F32) * dqcat[i][:, j * HD:(j + 1) * HD] for j in range(NSUB))
            for i in range(n)]
    ecend = [jnp.exp(ce) for ce in t["cend"]]
    end = [ecend[i] * _colsum(st0s[i] * dst1s[i]) + _colsum(ks[i] * dk_inter[i]) for i in range(n)]
    sums = _running_sums(dcum, [not fw for fw in fws])
    dg = [sm + en for sm, en in zip(sums, end)]
    upd = [_dot_tn(dos[i], qs[i] * e[i][0]) for i in range(n)]
    dst0 = [dst1s[i] * ecend[i] + upd[i] for i in range(n)]
    return dq, dk, dv, dg, dst0


def _chunk_index(step, n_ctx_chunks, n_chunks, fw):
    if fw:
        return step
    return jnp.where(step < n_ctx_chunks, n_ctx_chunks - 1 - step, n_chunks - 1 + n_ctx_chunks - step)


def _hg_inputs(hq, hf, lbv, d_idx, sl):
    lb = _sigmoid(lbv[d_idx:d_idx + 1, sl] - lbv[2 + d_idx:3 + d_idx, sl])
    sg = _sigmoid(hf)
    f = lb + (1.0 - lb) * sg
    return _silu(hq), 1.0 - f, jnp.log(f), f, sg, lb


def _scan_fwd_both(p, branch_sides, n_ctx_chunks, name):
    rows = p.shape[0]
    n_chunks = rows // CHUNK
    n_ins = [4 if branch == "hg" else 6 for branch, _ in branch_sides]
    n_in_all = 2 * sum(n_ins)
    n_br = len(branch_sides)

    def body(*refs):
        ins, outs, state = refs[:n_in_all], refs[n_in_all:n_in_all + 4 * n_br], refs[-1]

        @pl.when(pl.program_id(0) == 0)
        def _():
            state[...] = jnp.zeros_like(state)

        lanes, where = [], []
        pos = 0
        for bi, (branch, _) in enumerate(branch_sides):
            hg = branch == "hg"
            n_in = n_ins[bi]
            for di, fw in enumerate((True, False)):
                r = ins[pos:pos + n_in]
                pos += n_in
                o_ref, st_ref = outs[4 * bi + 2 * di], outs[4 * bi + 2 * di + 1]
                if hg:
                    a_ref, b_ref, c_ref, lb_ref = r
                else:
                    a_ref, b_ref, c_ref, lr_ref, wgk_ref, bgk_ref = r
                    logits = _dot(lr_ref[...], wgk_ref[...]) + bgk_ref[...]
                    g_all = _log_sigmoid(logits) * (1.0 / GATE_NORM)
                for h in range(NH):
                    sl = slice(h * HD, (h + 1) * HD)
                    if hg:
                        q, k, g, _, _, _ = _hg_inputs(a_ref[:, sl], c_ref[:, sl], lb_ref[...], di, sl)
                        v = b_ref[:, sl]
                    else:
                        q, k, v, g = a_ref[:, sl] * (HD ** -0.5), b_ref[:, sl], c_ref[:, sl], g_all[:, sl]
                    lanes.append((q, k, v, g, state[2 * bi + di, h], fw))
                    where.append((2 * bi + di, h, sl, o_ref, st_ref))
        qs, ks, vs, gs, st0s, fws = (list(col) for col in zip(*lanes))
        os_, st1s = _chunks_fwd(qs, ks, vs, gs, st0s, fws)
        for (si, h, sl, o_ref, st_ref), st0, o, st1 in zip(where, st0s, os_, st1s):
            st_ref[0, h] = st0
            o_ref[:, sl] = o
            state[si, h] = st1

    fixed = lambda j: (0, 0)
    in_specs, args, out_specs = [], [], []
    for branch, side in branch_sides:
        for di, fw in enumerate((True, False)):
            chunk = functools.partial(_chunk_index, n_ctx_chunks=n_ctx_chunks, n_chunks=n_chunks, fw=fw)

            def cmap(blk, width=HW, chunk=chunk):
                return pl.BlockSpec((CHUNK, width), lambda j: (chunk(j), blk))

            if branch == "hg":
                in_specs += [cmap(C_HQ), cmap(C_HI), cmap(C_HF_FW + di), pl.BlockSpec((4, HW), fixed)]
                args += [p, p, p, side]
            else:
                in_specs += [cmap(C_GQ), cmap(C_GK), cmap(C_GV), cmap(OFF_LR // 128, 128),
                             pl.BlockSpec((128, HW), fixed), pl.BlockSpec((1, HW), fixed)]
                args += [p, p, p, p, side[di][0], side[di][1]]
            out_specs += [cmap(0), pl.BlockSpec((1, NH, HD, HD), lambda j, chunk=chunk: (chunk(j), 0, 0, 0))]
    return pl.pallas_call(
        body, name=name, grid=(n_chunks,),
        out_shape=[jax.ShapeDtypeStruct((rows, HW), F32),
                   jax.ShapeDtypeStruct((n_chunks, NH, HD, HD), F32)] * (2 * n_br),
        in_specs=in_specs, out_specs=out_specs,
        scratch_shapes=[pltpu.VMEM((2 * n_br, NH, HD, HD), F32)],
        compiler_params=_cparams(dimension_semantics=("arbitrary",)),
    )(*args)


def _scan_bwd_both(p, branch_items, n_ctx_chunks, name):
    rows = p.shape[0]
    n_chunks = rows // CHUNK
    n_ins = [6 if item[0] == "hg" else 8 for item in branch_items]
    n_outs = [4 if item[0] == "hg" else 6 for item in branch_items]
    n_in_all, n_out_all = 2 * sum(n_ins), 2 * sum(n_outs)

    def body(*refs):
        ins, outs, dstate = refs[:n_in_all], refs[n_in_all:n_in_all + n_out_all], refs[-1]
        first = pl.program_id(0) == 0

        @pl.when(first)
        def _():
            dstate[...] = jnp.zeros_like(dstate)

        lanes, where, extra, ctx = [], [], [], []
        ipos = opos = 0
        for bi, item in enumerate(branch_items):
            hg = item[0] == "hg"
            for di, fw in enumerate((True, False)):
                r, w = ins[ipos:ipos + n_ins[bi]], outs[opos:opos + n_outs[bi]]
                ipos += n_ins[bi]
                opos += n_outs[bi]
                if hg:
                    a_ref, b_ref, c_ref, lb_ref, st_ref, do_ref = r
                    acc_refs = (w[3],)
                else:
                    a_ref, b_ref, c_ref, lr_ref, wgk_ref, bgk_ref, st_ref, do_ref = r
                    acc_refs = (w[4], w[5])
                    lr = lr_ref[...]
                    logits = _dot(lr, wgk_ref[...]) + bgk_ref[...]
                    g_all = _log_sigmoid(logits) * (1.0 / GATE_NORM)

                @pl.when(first)
                def _(acc_refs=acc_refs):
                    for ref in acc_refs:
                        ref[...] = jnp.zeros_like(ref)

                for h in range(NH):
                    sl = slice(h * HD, (h + 1) * HD)
                    if hg:
                        hq, hf = a_ref[:, sl], c_ref[:, sl]
                        q, k, g, f, sg, lb = _hg_inputs(hq, hf, lb_ref[...], di, sl)
                        v = b_ref[:, sl]
                        extra.append((hq, f, sg, lb))
                    else:
                        q, k, v, g = a_ref[:, sl] * (HD ** -0.5), b_ref[:, sl], c_ref[:, sl], g_all[:, sl]
                        extra.append(None)
                    lanes.append((q, k, v, g, st_ref[0, h], do_ref[:, sl], dstate[2 * bi + di, h], fw))
                    where.append((2 * bi + di, h, sl))
                ctx.append((hg, w, None if hg else (lr, logits, wgk_ref)))

        dqs, dks, dvs, dgs, dst0s = [], [], [], [], []
        for lo in range(0, len(lanes), BWD_LANES):
            cols = [list(col) for col in zip(*lanes[lo:lo + BWD_LANES])]
            for acc, part in zip((dqs, dks, dvs, dgs, dst0s), _chunks_bwd(*cols)):
                acc.extend(part)
        dg_parts = [[] for _ in ctx]
        for (si, h, sl), ex, dq, dk, dv, dg, dst0 in zip(where, extra, dqs, dks, dvs, dgs, dst0s):
            dstate[si, h] = dst0
            hg, w, _ = ctx[si]
            if hg:
                hq, f, sg, lb = ex
                da_ref, db_ref, dc_ref, dlb_ref = w
                da_ref[:, sl] = (dq * _dsilu(hq)).astype(BF16)
                db_ref[:, sl] = dv.astype(BF16)
                df = dg / f - dk
                dc_ref[:, sl] = (df * (1.0 - lb) * sg * (1.0 - sg)).astype(BF16)
                dlb_ref[0:1, sl] += _colsum(df * (1.0 - sg))
            else:
                da_ref, db_ref, dc_ref = w[:3]
                da_ref[:, sl] = (dq * (HD ** -0.5)).astype(BF16)
                db_ref[:, sl] = dk.astype(BF16)
                dc_ref[:, sl] = dv.astype(BF16)
                dg_parts[si].append(dg)
        for si, (hg, w, more) in enumerate(ctx):
            if not hg:
                dlr_ref, dwgk_ref, dbias_ref = w[3:]
                lr, logits, wgk_ref = more
                dlogits = jnp.concatenate(dg_parts[si], axis=1) * (1.0 / GATE_NORM) * (1.0 - _sigmoid(logits))
                dlr_ref[...] = _dot_nt(dlogits, wgk_ref[...]).astype(BF16)
                dwgk_ref[...] += _dot_tn(lr, dlogits)
                dbias_ref[0:1, :] += _colsum(dlogits)

    fixed = lambda j: (0, 0)
    big = jax.ShapeDtypeStruct((rows, HW), BF16)
    in_specs, args, out_shape, out_specs = [], [], [], []
    for branch, side, states, d_o in branch_items:
        for di, fw in enumerate((True, False)):
            def chunk_of(j, fw=fw):
                return _chunk_index(n_chunks - 1 - j, n_ctx_chunks, n_chunks, fw)

            def cmap(blk, width=HW, chunk_of=chunk_of):
                return pl.BlockSpec((CHUNK, width), lambda j: (chunk_of(j), blk))

            st_spec = pl.BlockSpec((1, NH, HD, HD), lambda j, chunk_of=chunk_of: (chunk_of(j), 0, 0, 0))
            if branch == "hg":
                in_specs += [cmap(C_HQ), cmap(C_HI), cmap(C_HF_FW + di), pl.BlockSpec((4, HW), fixed), st_spec,
                             cmap(0)]
                args += [p, p, p, side, states[di], d_o]
                out_shape += [big, big, big, jax.ShapeDtypeStruct((8, HW), F32)]
                out_specs += [cmap(0), cmap(0), cmap(0), pl.BlockSpec((8, HW), fixed)]
            else:
                in_specs += [cmap(C_GQ), cmap(C_GK), cmap(C_GV), cmap(OFF_LR // 128, 128),
                             pl.BlockSpec((128, HW), fixed), pl.BlockSpec((1, HW), fixed), st_spec, cmap(0)]
                args += [p, p, p, p, side[di][0], side[di][1], states[di], d_o]
                out_shape += [big, big, big, jax.ShapeDtypeStruct((rows, 128), BF16),
                              jax.ShapeDtypeStruct((128, HW), F32), jax.ShapeDtypeStruct((8, HW), F32)]
                out_specs += [cmap(0), cmap(0), cmap(0), cmap(0, 128), pl.BlockSpec((128, HW), fixed),
                              pl.BlockSpec((8, HW), fixed)]
    return pl.pallas_call(
        body, name=name, grid=(n_chunks,), out_shape=out_shape, in_specs=in_specs, out_specs=out_specs,
        scratch_shapes=[pltpu.VMEM((2 * len(branch_items), NH, HD, HD), F32)],
        compiler_params=_cparams(dimension_semantics=("arbitrary",)),
    )(*args)


SMALL_ROWS = 56
ROWS_MOD_X = (0, 1, 8, 16, 17, 18)
ROWS_MOD_C = (2, 3)
ROW_PRE1, ROW_POST1, ROW_ONORM, ROW_PRE2, ROW_POST2, ROW_LB, ROW_BGK, ROW_WGK = 4, 9, 10, 19, 20, 24, 32, 40
ROW_LOSS = 21


def _reduce_small(gathered, lb_full, name, after=()):
    _, _, d = gathered.shape

    def body(g_ref, lb_ref, *rest):
        sum_ref, dmod_ref, dbmod_ref, dlb_ref = rest[len(after):]
        total = g_ref[0]
        for b in range(1, N_DEV):
            total = total + g_ref[b]
        sum_ref[...] = total
        dmod_ref[...] = jnp.zeros_like(dmod_ref)
        for m in range(N_MOD):
            col = slice(m * d, (m + 1) * d)
            acc = jnp.zeros((1, d), F32)
            for b in range(N_DEV):
                row = g_ref[b, ROWS_MOD_X[m]:ROWS_MOD_X[m] + 1, :]
                dmod_ref[b:b + 1, col] = row
                acc = acc + row
            if m < 2:
                ctx_row = total[ROWS_MOD_C[m]:ROWS_MOD_C[m] + 1, :]
                dmod_ref[8:9, col] = ctx_row
                acc = acc + ctx_row
            dbmod_ref[:, col] = acc
        lbv = lb_ref[...]
        for dd in range(2):
            lb = _sigmoid(lbv[dd:dd + 1, :] - lbv[2 + dd:3 + dd, :])
            gl = total[ROW_LB:ROW_LB + 1, dd * HW:(dd + 1) * HW] * lb * (1.0 - lb)
            dlb_ref[dd:dd + 1, :] = gl
            dlb_ref[2 + dd:3 + dd, :] = -gl

    return pl.pallas_call(
        body, name=name,
        out_shape=[jax.ShapeDtypeStruct((SMALL_ROWS, d), F32), jax.ShapeDtypeStruct((16, N_MOD * d), F32),
                   jax.ShapeDtypeStruct((1, N_MOD * d), F32), jax.ShapeDtypeStruct((4, HW), F32)],
        in_specs=[VMEM_SPEC] * 2 + [ANY_SPEC] * len(after), out_specs=[VMEM_SPEC] * 4, compiler_params=_cparams(),
    )(gathered, lb_full, *after)


def _c_ctx_grad(gathered, c_ctx_row, name):
    def body(g_ref, c_ref, o_ref):
        acc = g_ref[0, 0:1, :]
        for chip in range(1, N_CHIP):
            acc = acc + g_ref[2 * chip, 0:1, :]
        o_ref[...] = acc * _dsilu(c_ref[...])

    return pl.pallas_call(
        body, name=name, out_shape=jax.ShapeDtypeStruct(c_ctx_row.shape, F32),
        in_specs=[VMEM_SPEC] * 2, out_specs=VMEM_SPEC, compiler_params=_cparams(),
    )(gathered, c_ctx_row)


def _blocked(full, n_blocks):
    k, n = full.shape
    return full.reshape(k, n_blocks, n // n_blocks).transpose(1, 0, 2)


def _unblocked(blocks):
    nb, k, n = blocks.shape
    return blocks.transpose(1, 0, 2).reshape(k, nb * n)


def _sample_front(x0, ctx0, modc, modx, norm_pre1, lb_full, gla_side, w_in_r):
    ctx_len = ctx0.shape[0]
    n_ctx_tiles = ctx_len // TM
    n_ctx_chunks = ctx_len // CHUNK
    h1, p = _in_projection(ctx0, x0, modc, modx, norm_pre1, w_in_r, n_ctx_tiles, "in_projection")
    (o_hg_fw, st_hg_fw, o_hg_bw, st_hg_bw, o_gla_fw, st_gla_fw, o_gla_bw, st_gla_bw) = _scan_fwd_both(
        p, [("hg", lb_full), ("gla", gla_side)], n_ctx_chunks, "scan_fwd")
    return dict(h1=h1, p=p, o_list=[o_hg_fw, o_hg_bw, o_gla_fw, o_gla_bw],
                states=[st_hg_fw, st_hg_bw, st_gla_fw, st_gla_bw])


def _sample_back(reduce, front, x0, ctx0, target0, modc, modx, norm_pre1, norms, onorms, lb_full, gla_side, w_in_r,
                 wbh, wbg, wout, ffn_weights):
    seq, d = x0.shape
    ctx_len = ctx0.shape[0]
    n_ctx_tiles = ctx_len // TM
    n_tiles = (ctx_len + seq) // TM
    n_ctx_chunks = ctx_len // CHUNK
    h1, p, o_list = front["h1"], front["p"], front["o_list"]
    st_hg_fw, st_hg_bw, st_gla_fw, st_gla_bw = front["states"]
    z2, y1, merged, og_hg, og_gla = _mixer_tail_fwd(x0, p, o_list, modx, norms, onorms, wbh, wbg, wout, n_ctx_tiles,
                                                    "mixer_tail")
    wg, wu, wd = ffn_weights([z2])
    loss_part, dz2, h2, a_act, du, dv, dy2, stat_ffn = _ffn_fwd_bwd(z2, modx, norms, wg, wu, wd, target0, "ffn")
    dff = wg.shape[0]
    tok = reduce("ffn", [_weight_grad(du, h2, "grad_w_ff_gate", tk=dff // 2, tn=d),
                         _weight_grad(dv, h2, "grad_w_ff_up", tk=dff // 2, tn=d),
                         _weight_grad(a_act, dy2, "grad_w_ff_down", tk=dff // 2)])

    (d_ohg, d_ogla, d_hgate, d_ggate, d_ghg, d_ggla, dy1, db_hg, db_gla, stat_mix) = _mixer_tail_bwd(
        x0, p, o_list, dz2, y1, modx + tok, norms, onorms, wbh, wbg, wout, n_ctx_tiles, n_tiles, "mixer_tail_bwd")
    tok = reduce("mix", [_weight_grad(og_hg, db_hg, "grad_w_br_hg"), _weight_grad(og_gla, db_gla, "grad_w_br_gla"),
                         _weight_grad(merged, dy1, "grad_w_out")])
    tok = tok + reduce("push_ffn", [dy1])
    gla_b = [(wgk, bias + tok) for wgk, bias in gla_side]
    (dgq_f, dgk_f, dgv_f, dlr_f, dwgk_f, dbgk_f, dgq_b, dgk_b, dgv_b, dlr_b, dwgk_b, dbgk_b,
     dhq_f, dhi_f, dhf_f, dlb_f, dhq_b, dhi_b, dhf_b, dlb_b) = _scan_bwd_both(
        p, [("gla", gla_b, (st_gla_fw, st_gla_bw), d_ogla), ("hg", lb_full, (st_hg_fw, st_hg_bw), d_ohg)],
        n_ctx_chunks, "scan_bwd")
    tok = reduce("push_mix", [dbgk_f])
    pieces = [dhq_f, dhq_b, dhi_f, dhi_b, dhf_f, dhf_b, d_hgate, dgq_f, dgq_b, dgk_f, dgk_b, dgv_f, dgv_b, d_ggate,
              d_ghg, d_ggla, dlr_f, dlr_b]
    dp, grad_x, stat_in = _in_projection_bwd(ctx0, x0, dz2, modc, modx, norm_pre1 + tok, w_in_r, pieces, n_ctx_tiles,
                                             "in_projection_bwd")

    started = reduce("small_start", dict(stat_in=stat_in, stat_mix=stat_mix, stat_ffn=stat_ffn, dlb=(dlb_f, dlb_b),
                                         dwgk=(dwgk_f, dwgk_b), dbgk=(dbgk_f, dbgk_b)))
    reduce("in", [_w_in_grad(dp, h1, w_in_r.shape[0], "grad_w_in", after=[started])])
    reduce("small", None)
    reduce("push_in", [])
    return dict(loss_part=loss_part, grad_x=grad_x)


def kernel(x, c, ctx, c_ctx, w_mod, b_mod, norm_pre1, norm_post1, norm_pre2, norm_post2, w_in, hg_lb, hg_onorm, gla_w_gk, gla_b_gk, gla_onorm, w_br_hg, w_br_gla, w_out, w_ff_gate, w_ff_up, w_ff_down, loss_target, m_c_ctx, m_w_mod, m_b_mod, m_norm_pre1, m_norm_post1, m_norm_pre2, m_norm_post2, m_w_in, m_hg_lb, m_hg_onorm, m_gla_w_gk, m_gla_b_gk, m_gla_onorm, m_w_br_hg, m_w_br_gla, m_w_out, m_w_ff_gate, m_w_ff_up, m_w_ff_down, v_c_ctx, v_w_mod, v_b_mod, v_norm_pre1, v_norm_post1, v_norm_pre2, v_norm_post2, v_w_in, v_hg_lb, v_hg_onorm, v_gla_w_gk, v_gla_b_gk, v_gla_onorm, v_w_br_hg, v_w_br_gla, v_w_out, v_w_ff_gate, v_w_ff_up, v_w_ff_down):
    seq, d = x.shape[1], x.shape[2]
    ctx_len = ctx.shape[1]
    assert seq % TM == 0 and ctx_len % TM == 0 and d == 2 * HW
    ax, ay, ac = lax.axis_index("x"), lax.axis_index("y"), lax.axis_index("c")
    chip = 2 * ax + ay
    dev = 2 * chip + ac
    c_arr = jnp.reshape(ac, (1,)).astype(jnp.int32)
    chip_arr = jnp.reshape(chip, (1,)).astype(jnp.int32)
    transposed = ("w_in", "w_ff_gate", "w_ff_up")
    view = lambda a, nm: a[0].T if nm in transposed else a[0]

    sems_in, lands_in, token_in0 = _blocks_start([_cast_into_blocks(chip_arr, view(w_in, "w_in"), "cast_w_in")],
                                                 "gather_w_in_start")

    nc = d // 128
    pad8 = lambda a: jnp.pad(a, ((0, -a.shape[0] % 8), (0, 0)))
    small1 = jnp.concatenate([c.reshape(nc, 128) + token_in0[0, 0], pad8(hg_lb.reshape(4, 128)),
                              gla_w_gk.reshape(2 * RANK, 128), pad8(gla_b_gk.reshape(2, 128))], axis=0)
    blocks = [_cast_into_blocks(chip_arr, view(w_, nm), "cast_" + nm) for w_, nm in (
        (w_br_hg, "w_br_hg"), (w_br_gla, "w_br_gla"), (w_out, "w_out"), (w_ff_gate, "w_ff_gate"),
        (w_ff_up, "w_ff_up"), (w_ff_down, "w_ff_down"))]
    got1 = _allgather8(small1, "gather_small_params", after=blocks)
    c_all = got1[:, :nc, :].reshape(N_DEV, d)
    per_chip = got1[0::2]
    lb_full = per_chip[:, nc:nc + 4, :].transpose(1, 0, 2).reshape(4, HW)
    wgk_full = per_chip[:, nc + 8:nc + 8 + 2 * RANK, :].transpose(1, 0, 2).reshape(2, RANK, HW)
    bgk_full = per_chip[:, nc + 8 + 2 * RANK:nc + 10 + 2 * RANK, :].transpose(1, 0, 2).reshape(2, HW)
    wgk_pad = [jnp.zeros((128, HW), F32).at[dd * RANK:(dd + 1) * RANK].set(wgk_full[dd]) for dd in range(2)]
    bgk = [bgk_full[dd:dd + 1] for dd in range(2)]

    n_mod_cols = w_mod.shape[2]
    cond = jnp.concatenate([c_all, pad8(c_ctx.reshape(1, d))], axis=0)
    b_cols = lax.dynamic_slice(b_mod, (0, chip * n_mod_cols), (1, n_mod_cols))
    lands_in = _blocks_wait(sems_in, lands_in, [got1], "gather_w_in_wait")
    fwd_sems, lands_in, fwd_token = _forward_start(lands_in, "gather_w_in_forward_start")
    mod_part = _mod_forward(cond + fwd_token[0, 0], w_mod[0], b_cols, "mod_forward")
    mod_got = _allgather8(mod_part, "gather_mod")
    mod_all = mod_got[0::2].transpose(1, 0, 2).reshape(16, N_CHIP * n_mod_cols)
    modx = pad8(lax.dynamic_slice(mod_all, (dev, 0), (1, N_MOD * d)).reshape(N_MOD, d))
    modc = pad8(mod_all[8].reshape(N_MOD, d))

    gathered_in = _forward_wait(fwd_sems, lands_in, [mod_got], "gather_w_in_forward_wait")
    sems, lands, token = _blocks_start(blocks, "gather_rest_start", after=[gathered_in[0]])
    w_in_r = gathered_in[0].reshape(-1, d)

    norms = jnp.concatenate([norm_pre1, norm_post1, norm_pre2, norm_post2, jnp.zeros((4, d), F32)], axis=0)
    onorms = jnp.zeros((8, d), F32).at[0, :HD].set(hg_onorm[0]).at[1, :HD].set(gla_onorm[0])
    gla_side = [(wgk_pad[dd], bgk[dd]) for dd in range(2)]
    modx = modx + token[0, 0]
    front = _sample_front(x[0], ctx[0], modc, modx, norm_pre1, lb_full, gla_side, w_in_r)
    lands = _blocks_wait(sems, lands, front["o_list"], "gather_rest_wait")
    gathered = _blocks_finish(lands[:3], "gather_mix_finish")
    wbh, wbg = _unblocked(gathered[0]), _unblocked(gathered[1])
    wout = gathered[2].reshape(d, d)
    ffn_sems, ffn_lands, ffn_token = _forward_start(lands[3:], "gather_ffn_forward_start")
    onorms = onorms + ffn_token[0, 0]

    def ffn_weights(after):
        got = _forward_wait(ffn_sems, ffn_lands, after, "gather_ffn_forward_wait")
        return tuple(g.reshape(-1, d) for g in got)

    dff = w_ff_down.shape[1] * N_CHIP
    groups = {"ffn": ["w_ff_gate", "w_ff_up", "w_ff_down"], "mix": ["w_br_hg", "w_br_gla", "w_out"], "in": ["w_in"]}
    row_sharded = {"w_out": d // N_CHIP, "w_ff_down": dff // N_CHIP, "w_ff_gate": dff // N_CHIP,
                   "w_ff_up": dff // N_CHIP, "w_in": w_in.shape[2]}
    in_flight, to_sibling, small = {}, {}, {}

    def reduce_small_start(stats):
        small2 = jnp.concatenate([
            stats["stat_in"], stats["stat_mix"], stats["stat_ffn"],
            jnp.concatenate(stats["dlb"], axis=1), jnp.concatenate(stats["dbgk"], axis=1),
            jnp.concatenate([stats["dwgk"][0][0:RANK], stats["dwgk"][1][RANK:2 * RANK]], axis=1)], axis=0)
        assert small2.shape[0] == SMALL_ROWS
        sems_, land_, token_ = _allgather8_start(small2, "gather_small_grads_start")
        small.update(gathering=(sems_, land_))
        return token_

    def reduce_small():
        got2 = small["gathered"]
        total, dmod_all, g_b_mod, g_lb_full = _reduce_small(got2, lb_full, "reduce_small", to_sibling["in"][1])
        dmod_cols = lax.dynamic_slice(dmod_all, (0, chip * n_mod_cols), (16, n_mod_cols))
        g_w_mod, cctx_part = _mod_backward(cond, w_mod[0], dmod_cols, "mod_backward")
        got3 = _allgather8(cctx_part, "gather_c_ctx_grad")
        g_c_ctx = _c_ctx_grad(got3, c_ctx.reshape(1, d), "c_ctx_grad")
        small.update(total=total, g_b_mod=g_b_mod, g_lb_full=g_lb_full, g_w_mod=g_w_mod, g_c_ctx=g_c_ctx)

    def reduce(group, grads):
        if group == "small_start":
            return reduce_small_start(grads)
        if group == "small":
            return reduce_small()
        if group.startswith("push_"):
            return push(group[5:], grads)
        nms = groups[group]
        after = []
        if group == "in":
            small.update(gathered=_allgather8_wait(*small["gathering"], grads, "gather_small_grads_wait"))
            after = [small["gathered"]]
        full = [g.reshape(N_CHIP, row_sharded[nm], d) if nm in row_sharded else _blocked(g, N_CHIP)
                for g, nm in zip(grads, nms)]
        sems_, full, lands_, token_ = _send_half_start(full, "grads_to_sibling_start_" + group, after)
        to_sibling[group] = (sems_, full, lands_)
        return token_[0, 0]

    def push(group, after):
        nms = groups[group]
        sems_, full, lands_ = to_sibling[group]
        if group == "in":
            after = list(after) + [small["g_c_ctx"], small["total"]]
        full, from_sibling = _send_half_wait(sems_, full, lands_, after, "grads_to_sibling_wait_" + group)
        pairs = [_pair_sum(c_arr, f, r_, "pair_sum_" + nm) for f, r_, nm in zip(full, from_sibling, nms)]
        after = [small["g_c_ctx"], small["total"]] if group == "in" else []
        sems_, pairs, lands_, token_ = _scatter_start(pairs, "grads_to_owner_start_" + group, after)
        in_flight[group] = (sems_, pairs, lands_, token_)
        return token_[0, 0]

    r = _sample_back(reduce, front, x[0], ctx[0], loss_target[0], modc, modx, norm_pre1, norms, onorms, lb_full,
                     gla_side, w_in_r, wbh, wbg, wout, ffn_weights)
    grad_x = r["grad_x"]

    weights = dict(w_in=(w_in, m_w_in, v_w_in), w_br_hg=(w_br_hg, m_w_br_hg, v_w_br_hg),
                   w_br_gla=(w_br_gla, m_w_br_gla, v_w_br_gla), w_out=(w_out, m_w_out, v_w_out),
                   w_ff_gate=(w_ff_gate, m_w_ff_gate, v_w_ff_gate), w_ff_up=(w_ff_up, m_w_ff_up, v_w_ff_up),
                   w_ff_down=(w_ff_down, m_w_ff_down, v_w_ff_down))
    names = ["w_in", "w_br_hg", "w_br_gla", "w_out", "w_ff_gate", "w_ff_up", "w_ff_down"]
    big, swapping = {}, {}

    def sum_and_swap(group, after):
        sems_, pairs, lands_, _ = in_flight[group]
        pairs, lands_ = _scatter_wait(sems_, pairs, lands_, after, "grads_to_owner_wait_" + group)
        own_half = [_sum_owner(chip_arr, pr, g, "chip_sum_" + nm) for pr, g, nm in zip(pairs, lands_, groups[group])]
        swapping[group] = _swap_start(own_half, "halves_to_sibling_start_" + group)
        return own_half[-1]

    def update(group, after):
        sems_, own_half, lands_ = swapping[group]
        own_half, other_half = _swap_wait(sems_, own_half, lands_, after, "halves_to_sibling_wait_" + group)
        done = []
        for nm, own, oth in zip(groups[group], own_half, other_half):
            w_, m_, v_ = (view(a, nm) for a in weights[nm])
            res = _adamw_halves(c_arr, own, oth, w_, m_, v_, "adamw_" + nm)
            big[nm] = [r_.T[None] if nm in transposed else r_[None] for r_ in res]
            done.append(res[1])
        return done

    token_in = in_flight["in"][3]
    summed_ffn = sum_and_swap("ffn", [token_in])
    summed_mix = sum_and_swap("mix", [summed_ffn])

    total, g_b_mod, g_lb_full, g_w_mod, g_c_ctx = (small[k] for k in ("total", "g_b_mod", "g_lb_full", "g_w_mod",
                                                                      "g_c_ctx"))
    g_pre1, g_post1, g_pre2, g_post2 = (total[r_:r_ + 1] for r_ in (ROW_PRE1, ROW_POST1, ROW_PRE2, ROW_POST2))
    g_hg_on, g_gla_on = total[ROW_ONORM:ROW_ONORM + 1, 0:HD], total[ROW_ONORM:ROW_ONORM + 1, HD:2 * HD]
    n_lb = hg_lb.shape[2]
    g_hg_lb = lax.dynamic_slice(g_lb_full, (0, chip * n_lb), (4, n_lb))
    g_bgk = lax.dynamic_slice(total[ROW_BGK:ROW_BGK + 1].reshape(2, HW), (0, chip * n_lb), (2, n_lb))
    g_wgk_full = total[ROW_WGK:ROW_WGK + RANK].reshape(RANK, 2, HW).transpose(1, 0, 2).reshape(2 * RANK, HW)
    g_wgk = lax.dynamic_slice(g_wgk_full, (0, chip * n_lb), (2 * RANK, n_lb))

    small_items = [
        (g_c_ctx, c_ctx.reshape(1, d), m_c_ctx.reshape(1, d), v_c_ctx.reshape(1, d)),
        (g_b_mod, b_mod, m_b_mod, v_b_mod),
        (g_pre1, norm_pre1, m_norm_pre1, v_norm_pre1),
        (g_post1, norm_post1, m_norm_post1, v_norm_post1),
        (g_pre2, norm_pre2, m_norm_pre2, v_norm_pre2),
        (g_post2, norm_post2, m_norm_post2, v_norm_post2),
        (g_hg_lb, hg_lb.reshape(4, n_lb), m_hg_lb.reshape(4, n_lb), v_hg_lb.reshape(4, n_lb)),
        (g_hg_on, hg_onorm, m_hg_onorm, v_hg_onorm),
        (g_wgk, gla_w_gk.reshape(2 * RANK, n_lb), m_gla_w_gk.reshape(2 * RANK, n_lb), v_gla_w_gk.reshape(2 * RANK, n_lb)),
        (g_bgk, gla_b_gk.reshape(2, n_lb), m_gla_b_gk.reshape(2, n_lb), v_gla_b_gk.reshape(2, n_lb)),
        (g_gla_on, gla_onorm, m_gla_onorm, v_gla_onorm),
    ]
    small_res = _adamw_whole(small_items, "adamw_small")
    mod_res = _adamw_tiled(g_w_mod, w_mod[0], m_w_mod[0], v_w_mod[0], "adamw_w_mod")
    done_ffn = update("ffn", [summed_mix, mod_res[0], small_res[0][0]])
    done_mix = update("mix", done_ffn)
    update("in", [sum_and_swap("in", done_mix)])

    loss = total[ROW_LOSS, 0]

    shapes = dict(c_ctx=c_ctx.shape, b_mod=b_mod.shape, norm_pre1=norm_pre1.shape, norm_post1=norm_post1.shape,
                  norm_pre2=norm_pre2.shape, norm_post2=norm_post2.shape, hg_lb=hg_lb.shape, hg_onorm=hg_onorm.shape,
                  gla_w_gk=gla_w_gk.shape, gla_b_gk=gla_b_gk.shape, gla_onorm=gla_onorm.shape)
    small_names = ["c_ctx", "b_mod", "norm_pre1", "norm_post1", "norm_pre2", "norm_post2", "hg_lb", "hg_onorm",
                   "gla_w_gk", "gla_b_gk", "gla_onorm"]
    grads, deltas, new_m, new_v = {}, {}, {}, {}
    for nm, item, res in zip(small_names, small_items, small_res):
        grads[nm] = item[0].reshape(shapes[nm])
        deltas[nm], new_m[nm], new_v[nm] = (r_.reshape(shapes[nm]) for r_ in res)
    grads["w_mod"] = g_w_mod[None]
    deltas["w_mod"], new_m["w_mod"], new_v["w_mod"] = (r_[None] for r_ in mod_res)
    for nm in names:
        grads[nm], deltas[nm], new_m[nm], new_v[nm] = big[nm]
    order = ["c_ctx", "w_mod", "b_mod", "norm_pre1", "norm_post1", "norm_pre2", "norm_post2", "w_in", "hg_lb",
             "hg_onorm", "gla_w_gk", "gla_b_gk", "gla_onorm", "w_br_hg", "w_br_gla", "w_out", "w_ff_gate", "w_ff_up",
             "w_ff_down"]
    return (loss, grad_x[None], *[grads[n] for n in order], *[deltas[n] for n in order],
            *[new_m[n] for n in order], *[new_v[n] for n in order])
```

```python
import functools

import jax
import jax.numpy as jnp
from jax import lax
from jax.experimental import pallas as pl
from jax.experimental.pallas import tpu as pltpu

F32 = jnp.float32
BF16 = jnp.bfloat16
MESH = pl.DeviceIdType.MESH

EPS = 1e-6
CHUNK = 64
SUB = 16
NSUB = CHUNK // SUB
NH = 4
HD = 128
HW = NH * HD
RANK = 16
GATE_NORM = 16.0
N_MOD = 6
TM = 256
BWD_LANES = 8
N_DEV = 8
N_CHIP = 4
VMEM_LIMIT = 56 * 1024 * 1024

ADAM_LR = 0.001
ADAM_B1 = 0.9
ADAM_B2 = 0.999
ADAM_EPS = 1e-08
ADAM_WD = 0.01
ADAM_STEP = 10

VMEM_SPEC = pl.BlockSpec(memory_space=pltpu.VMEM)
ANY_SPEC = pl.BlockSpec(memory_space=pl.ANY)
HBM_SPEC = pl.BlockSpec(memory_space=pltpu.HBM)
SEM_SPEC = pl.BlockSpec(memory_space=pltpu.SEMAPHORE)
EFFECT = pltpu.SideEffectType.DATAFLOW_SIDE_EFFECTING


def _cparams(**kw):
    return pltpu.CompilerParams(vmem_limit_bytes=VMEM_LIMIT, **kw)


def _dot(a, b):
    return jnp.dot(a.astype(BF16), b.astype(BF16), preferred_element_type=F32)


def _dot_nt(a, b):
    return lax.dot_general(a.astype(BF16), b.astype(BF16), (((1,), (1,)), ((), ())), preferred_element_type=F32)


def _dot_tn(a, b):
    return lax.dot_general(a.astype(BF16), b.astype(BF16), (((0,), (0,)), ((), ())), preferred_element_type=F32)


def _sigmoid(x):
    return 1.0 / (1.0 + jnp.exp(-x))


def _silu(x):
    return x * _sigmoid(x)


def _dsilu(x):
    s = _sigmoid(x)
    return s * (1.0 + x * (1.0 - s))


def _log_sigmoid(x):
    return jnp.minimum(x, 0.0) - jnp.log(1.0 + jnp.exp(-jnp.abs(x)))


def _colsum(a):
    return jnp.sum(a, axis=0, keepdims=True)


def _rms(a):
    r = lax.rsqrt(jnp.mean(a * a, axis=-1, keepdims=True) + EPS)
    return a * r, r


def _rms_bwd(dn, n, r):
    return r * (dn - n * jnp.mean(dn * n, axis=-1, keepdims=True))


def _place():
    x, y, c = lax.axis_index("x"), lax.axis_index("y"), lax.axis_index("c")
    chips = [(1 - x, y), (x, 1 - y), (1 - x, 1 - y)]
    return x, y, c, chips


def _allgather8(v, name, after=()):
    rows, cols = v.shape
    n_after = len(after)

    def body(x_ref, *rest):
        out_ref, send_sems, recv_sems, local_sem = rest[n_after:]
        x, y, c, chips = _place()
        me, sibling = (x, y, c), (x, y, 1 - c)

        def blk(px, py, pc):
            return out_ref.at[4 * px + 2 * py + pc]

        def copy(k, block, to, src=None):
            return pltpu.make_async_remote_copy(
                src_ref=blk(*block) if src is None else src, dst_ref=blk(*block),
                send_sem=send_sems.at[k], recv_sem=recv_sems.at[k], device_id=to, device_id_type=MESH)

        mine = pltpu.make_async_copy(x_ref, blk(*me), local_sem)
        mine.start()
        first = [copy(0, me, sibling, src=x_ref)]
        first += [copy(1 + j, me, (*chip, c), src=x_ref) for j, chip in enumerate(chips)]
        for cp in first:
            cp.start()
        passed = [copy(4 + j, (*chip, c), sibling) for j, chip in enumerate(chips)]
        for j, chip in enumerate(chips):
            copy(1 + j, (*chip, c), me).wait_recv()
            passed[j].start()
        copy(0, sibling, me).wait_recv()
        for j, chip in enumerate(chips):
            copy(4 + j, (*chip, 1 - c), me).wait_recv()
        for cp in first + passed:
            cp.wait_send()
        mine.wait()

    return pl.pallas_call(
        body, name=name,
        out_shape=jax.ShapeDtypeStruct((N_DEV, rows, cols), v.dtype),
        in_specs=[VMEM_SPEC] + [ANY_SPEC] * n_after, out_specs=VMEM_SPEC,
        scratch_shapes=[pltpu.SemaphoreType.DMA((7,)), pltpu.SemaphoreType.DMA((7,)), pltpu.SemaphoreType.DMA],
    )(v, *after)


def _allgather8_start(v, name):
    rows, cols = v.shape

    def body(x_ref, out_ref, *rest):
        send_sems, recv_sems, token, local_sem = rest[:4], rest[4:8], rest[-2], rest[-1]
        x, y, c, chips = _place()
        own = out_ref.at[4 * x + 2 * y + c]
        mine = pltpu.make_async_copy(x_ref, own, local_sem)
        mine.start()
        mine.wait()
        for k, to in enumerate([(x, y, 1 - c)] + [(*chip, c) for chip in chips]):
            pltpu.make_async_remote_copy(src_ref=own, dst_ref=own, send_sem=send_sems[k], recv_sem=recv_sems[k],
                                         device_id=to, device_id_type=MESH).start()
        token[...] = jnp.zeros_like(token)

    land = _hbm(lax.empty((N_DEV, rows, cols), v.dtype))
    out = pl.pallas_call(
        body, name=name,
        out_shape=(*[pltpu.SemaphoreType.DMA(())] * 8, pltpu.HBM(land.shape, land.dtype),
                   jax.ShapeDtypeStruct((8, 128), F32)),
        in_specs=[VMEM_SPEC, HBM_SPEC],
        out_specs=(*[SEM_SPEC] * 8, HBM_SPEC, VMEM_SPEC),
        input_output_aliases={1: 8},
        scratch_shapes=[pltpu.SemaphoreType.DMA],
        compiler_params=pltpu.CompilerParams(has_side_effects=EFFECT),
    )(v, land)
    return list(out[:8]), out[8], out[9]


def _allgather8_wait(sems, land, after, name):
    def blk(ref, px, py, pc):
        return ref.at[4 * px + 2 * py + pc]

    def wait_body(out_ref, *rest):
        send_sems, recv_sems = rest[:4], rest[4:8]
        x, y, c, chips = _place()
        me = (x, y, c)
        for k, peer in enumerate([(x, y, 1 - c)] + [(*chip, c) for chip in chips]):
            sent = pltpu.make_async_remote_copy(
                src_ref=blk(out_ref, *me), dst_ref=blk(out_ref, *me), send_sem=send_sems[k], recv_sem=recv_sems[k],
                device_id=peer, device_id_type=MESH)
            sent.wait_send()
            pltpu.make_async_remote_copy(
                src_ref=blk(out_ref, *peer), dst_ref=blk(out_ref, *peer), send_sem=send_sems[k],
                recv_sem=recv_sems[k], device_id=me, device_id_type=MESH).wait_recv()

    def pass_body(out_ref, _, send_sems, recv_sems):
        x, y, c, chips = _place()
        started = []
        for j, chip in enumerate(chips):
            cp = pltpu.make_async_remote_copy(
                src_ref=blk(out_ref, *chip, c), dst_ref=blk(out_ref, *chip, c), send_sem=send_sems.at[j],
                recv_sem=recv_sems.at[j], device_id=(x, y, 1 - c), device_id_type=MESH)
            cp.start()
            started.append(cp)
        for j, chip in enumerate(chips):
            pltpu.make_async_remote_copy(
                src_ref=blk(out_ref, *chip, 1 - c), dst_ref=blk(out_ref, *chip, 1 - c), send_sem=send_sems.at[j],
                recv_sem=recv_sems.at[j], device_id=(x, y, c), device_id_type=MESH).wait_recv()
        for cp in started:
            cp.wait_send()

    land = pl.pallas_call(
        wait_body, name=name,
        out_shape=pltpu.HBM(land.shape, land.dtype),
        in_specs=[HBM_SPEC] + [SEM_SPEC] * 8 + [ANY_SPEC] * len(after),
        out_specs=HBM_SPEC,
        input_output_aliases={0: 0},
        compiler_params=pltpu.CompilerParams(has_side_effects=EFFECT),
    )(land, *sems, *after)
    return pl.pallas_call(
        pass_body, name=name + "_pass",
        out_shape=jax.ShapeDtypeStruct(land.shape, land.dtype),
        in_specs=[ANY_SPEC], out_specs=ANY_SPEC,
        input_output_aliases={0: 0},
        scratch_shapes=[pltpu.SemaphoreType.DMA((3,)), pltpu.SemaphoreType.DMA((3,))],
    )(land)


def _cast_into_blocks(chip_arr, w, name):
    rows, cols = w.shape
    tr = _row_tile(rows, 16, 256)

    def body(chip_ref, w_ref, o_ref):
        o_ref[0] = w_ref[...].astype(BF16)

    return pl.pallas_call(
        body, name=name,
        grid_spec=pltpu.PrefetchScalarGridSpec(
            num_scalar_prefetch=1, grid=(rows // tr,),
            in_specs=[pl.BlockSpec((tr, cols), lambda i, chip_ref: (i, 0))],
            out_specs=pl.BlockSpec((1, tr, cols), lambda i, chip_ref: (chip_ref[0], i, 0))),
        out_shape=jax.ShapeDtypeStruct((N_CHIP, rows, cols), BF16),
        compiler_params=_cparams(dimension_semantics=("parallel",)),
    )(chip_arr, w)


def _halved_by_rows(shape):
    return (shape[1] // 2) % 16 == 0


def _half_of(ref, pc, block=None):
    lead = slice(None) if block is None else block
    if _halved_by_rows(ref.shape):
        h = ref.shape[1] // 2
        return ref.at[lead, pl.ds(pl.multiple_of(pc * h, 16), h), :]
    h = ref.shape[2] // 2
    return ref.at[lead, :, pl.ds(pl.multiple_of(pc * h, 128), h)]


def _half_shape(shape):
    return (shape[0], shape[1] // 2, shape[2]) if _halved_by_rows(shape) else (shape[0], shape[1], shape[2] // 2)


def _half_rows(ref, chip_id, pc):
    return _half_of(ref, pc, chip_id)


def _hbm(a):
    return pltpu.with_memory_space_constraint(a, pltpu.HBM)


def _blocks_start(lands, name, after=()):
    n = len(lands)
    n_sem = 3 * n
    first = n + len(after)

    def body(*refs):
        lnd = refs[:n]
        send_sems, recv_sems = refs[first:first + n_sem], refs[first + n_sem:first + 2 * n_sem]
        token = refs[-1]
        x, y, c, chips = _place()
        me_chip = 2 * x + y
        for k in range(n):
            for j, chip in enumerate(chips):
                pltpu.make_async_remote_copy(
                    src_ref=_half_rows(lnd[k], me_chip, c), dst_ref=_half_rows(lnd[k], me_chip, c),
                    send_sem=send_sems[3 * k + j], recv_sem=recv_sems[3 * k + j],
                    device_id=(*chip, c), device_id_type=MESH).start()
        token[...] = jnp.zeros_like(token)

    out = pl.pallas_call(
        body, name=name,
        out_shape=(*[pltpu.SemaphoreType.DMA(())] * (2 * n_sem),
                   *[pltpu.HBM(l.shape, l.dtype) for l in lands],
                   jax.ShapeDtypeStruct((8, 128), F32)),
        in_specs=[HBM_SPEC] * n + [ANY_SPEC] * len(after),
        out_specs=(*[SEM_SPEC] * (2 * n_sem), *[HBM_SPEC] * n, VMEM_SPEC),
        input_output_aliases={i: 2 * n_sem + i for i in range(n)},
        compiler_params=pltpu.CompilerParams(has_side_effects=EFFECT),
    )(*[_hbm(l) for l in lands], *after)
    return list(out[:2 * n_sem]), list(out[2 * n_sem:2 * n_sem + n]), out[-1]


def _blocks_wait(sems, lands, after, name):
    n = len(lands)
    n_sem = 3 * n

    def body(*refs):
        lnd = refs[:n]
        s_sems, r_sems = refs[n:n + n_sem], refs[n + n_sem:n + 2 * n_sem]
        x, y, c, chips = _place()
        me_chip = 2 * x + y
        for k in range(n):
            for j, (px, py) in enumerate(chips):
                cp = pltpu.make_async_remote_copy(
                    src_ref=_half_rows(lnd[k], me_chip, c), dst_ref=_half_rows(lnd[k], 2 * px + py, c),
                    send_sem=s_sems[3 * k + j], recv_sem=r_sems[3 * k + j],
                    device_id=(px, py, c), device_id_type=MESH)
                cp.wait_send()
                cp.wait_recv()

    out = pl.pallas_call(
        body, name=name,
        out_shape=tuple(pltpu.HBM(l.shape, l.dtype) for l in lands),
        in_specs=[HBM_SPEC] * n + [SEM_SPEC] * (2 * n_sem) + [ANY_SPEC] * len(after),
        out_specs=[HBM_SPEC] * n,
        input_output_aliases={i: i for i in range(n)},
        compiler_params=pltpu.CompilerParams(has_side_effects=EFFECT),
    )(*lands, *sems, *after)
    return list(out)


def _forward_start(lands, name):
    n = len(lands)
    n_sem = 3 * n

    def body(*refs):
        lnd = refs[:n]
        send_sems, recv_sems = refs[n:n + n_sem], refs[n + n_sem:n + 2 * n_sem]
        x, y, c, chips = _place()
        for k in range(n):
            for j, (px, py) in enumerate(chips):
                pltpu.make_async_remote_copy(
                    src_ref=_half_rows(lnd[k], 2 * px + py, c), dst_ref=_half_rows(lnd[k], 2 * px + py, c),
                    send_sem=send_sems[3 * k + j], recv_sem=recv_sems[3 * k + j],
                    device_id=(x, y, 1 - c), device_id_type=MESH).start()
        refs[-1][...] = jnp.zeros_like(refs[-1])

    out = pl.pallas_call(
        body, name=name,
        out_shape=(*[pltpu.SemaphoreType.DMA(())] * (2 * n_sem), *[pltpu.HBM(l.shape, l.dtype) for l in lands],
                   jax.ShapeDtypeStruct((8, 128), F32)),
        in_specs=[HBM_SPEC] * n,
        out_specs=(*[SEM_SPEC] * (2 * n_sem), *[HBM_SPEC] * n, VMEM_SPEC),
        input_output_aliases={i: 2 * n_sem + i for i in range(n)},
        compiler_params=pltpu.CompilerParams(has_side_effects=EFFECT),
    )(*[_hbm(l) for l in lands])
    return list(out[:2 * n_sem]), list(out[2 * n_sem:2 * n_sem + n]), out[-1]


def _forward_wait(sems, lands, after, name):
    n = len(lands)
    n_sem = 3 * n

    def body(*refs):
        lnd = refs[:n]
        s_sems, r_sems = refs[n:n + n_sem], refs[n + n_sem:n + 2 * n_sem]
        x, y, c, chips = _place()
        for k in range(n):
            for j, (px, py) in enumerate(chips):
                cp = pltpu.make_async_remote_copy(
                    src_ref=_half_rows(lnd[k], 2 * px + py, c), dst_ref=_half_rows(lnd[k], 2 * px + py, 1 - c),
                    send_sem=s_sems[3 * k + j], recv_sem=r_sems[3 * k + j],
                    device_id=(x, y, 1 - c), device_id_type=MESH)
                cp.wait_send()
                cp.wait_recv()

    out = pl.pallas_call(
        body, name=name,
        out_shape=tuple(pltpu.HBM(l.shape, l.dtype) for l in lands),
        in_specs=[HBM_SPEC] * n + [SEM_SPEC] * (2 * n_sem) + [ANY_SPEC] * len(after),
        out_specs=[HBM_SPEC] * n,
        input_output_aliases={i: i for i in range(n)},
        compiler_params=pltpu.CompilerParams(has_side_effects=EFFECT),
    )(*lands, *sems, *after)
    return list(out)


def _blocks_finish(lands, name):
    n = len(lands)

    def body(*refs):
        lnd = refs[n:2 * n]
        send_sems, recv_sems = refs[2 * n:]
        x, y, c, chips = _place()
        sibling = (x, y, 1 - c)

        def copy(k, j, chip_id, pc):
            return pltpu.make_async_remote_copy(
                src_ref=_half_rows(lnd[k], chip_id, pc), dst_ref=_half_rows(lnd[k], chip_id, pc),
                send_sem=send_sems.at[k, j], recv_sem=recv_sems.at[k, j], device_id=sibling, device_id_type=MESH)

        started = []
        for k in range(n):
            for j, (px, py) in enumerate(chips):
                cp = copy(k, j, 2 * px + py, c)
                cp.start()
                started.append(cp)
        for k in range(n):
            for j, (px, py) in enumerate(chips):
                copy(k, j, 2 * px + py, 1 - c).wait_recv()
        for cp in started:
            cp.wait_send()

    out = pl.pallas_call(
        body, name=name,
        out_shape=[jax.ShapeDtypeStruct(l.shape, l.dtype) for l in lands],
        in_specs=[ANY_SPEC] * n, out_specs=[ANY_SPEC] * n,
        input_output_aliases={i: i for i in range(n)},
        scratch_shapes=[pltpu.SemaphoreType.DMA((n, 3)), pltpu.SemaphoreType.DMA((n, 3))],
    )(*lands)
    return list(out)


def _send_half_start(arrs, name, after=()):
    n = len(arrs)
    first = 2 * n + len(after)

    def body(*refs):
        ins, lnd = refs[:n], refs[n:2 * n]
        send_sems, recv_sems = refs[first:first + n], refs[first + n:first + 2 * n]
        token = refs[-1]
        x, y, c, _ = _place()
        for k in range(n):
            pltpu.make_async_remote_copy(
                src_ref=_half_of(ins[k], 1 - c), dst_ref=lnd[k], send_sem=send_sems[k], recv_sem=recv_sems[k],
                device_id=(x, y, 1 - c), device_id_type=MESH).start()
        token[...] = jnp.zeros_like(token)

    lands = [_hbm(lax.empty(_half_shape(a.shape), a.dtype)) for a in arrs]
    out = pl.pallas_call(
        body, name=name,
        out_shape=(*[pltpu.SemaphoreType.DMA(())] * (2 * n), *[pltpu.HBM(a.shape, a.dtype) for a in arrs],
                   *[pltpu.HBM(l.shape, l.dtype) for l in lands], jax.ShapeDtypeStruct((8, 128), F32)),
        in_specs=[HBM_SPEC] * (2 * n) + [ANY_SPEC] * len(after),
        out_specs=(*[SEM_SPEC] * (2 * n), *[HBM_SPEC] * (2 * n), VMEM_SPEC),
        input_output_aliases={i: 2 * n + i for i in range(2 * n)},
        compiler_params=pltpu.CompilerParams(has_side_effects=EFFECT),
    )(*[_hbm(a) for a in arrs], *lands, *after)
    return list(out[:2 * n]), list(out[2 * n:3 * n]), list(out[3 * n:4 * n]), out[-1]


def _send_half_wait(sems, arrs, lands, after, name):
    n = len(arrs)

    def body(*refs):
        ins, lnd = refs[:n], refs[n:2 * n]
        s_sems, r_sems = refs[2 * n:3 * n], refs[3 * n:4 * n]
        x, y, c, _ = _place()
        for k in range(n):
            cp = pltpu.make_async_remote_copy(
                src_ref=_half_of(ins[k], 1 - c), dst_ref=lnd[k], send_sem=s_sems[k], recv_sem=r_sems[k],
                device_id=(x, y, 1 - c), device_id_type=MESH)
            cp.wait_send()
            cp.wait_recv()

    out = pl.pallas_call(
        body, name=name,
        out_shape=tuple(pltpu.HBM(a.shape, a.dtype) for a in list(arrs) + list(lands)),
        in_specs=[HBM_SPEC] * (2 * n) + [SEM_SPEC] * (2 * n) + [ANY_SPEC] * len(after),
        out_specs=[HBM_SPEC] * (2 * n),
        input_output_aliases={i: i for i in range(2 * n)},
        compiler_params=pltpu.CompilerParams(has_side_effects=EFFECT),
    )(*arrs, *lands, *sems, *after)
    return list(out[:n]), list(out[n:])


def _scatter_start(arrs, name, after=()):
    n = len(arrs)
    n_sem = 3 * n
    first = 2 * n + len(after)

    def body(*refs):
        ins, lnd = refs[:n], refs[n:2 * n]
        send_sems, recv_sems = refs[first:first + n_sem], refs[first + n_sem:first + 2 * n_sem]
        token = refs[-1]
        x, y, c, chips = _place()
        me_chip = 2 * x + y
        for k in range(n):
            for j, (px, py) in enumerate(chips):
                pltpu.make_async_remote_copy(
                    src_ref=ins[k].at[2 * px + py], dst_ref=lnd[k].at[me_chip],
                    send_sem=send_sems[3 * k + j], recv_sem=recv_sems[3 * k + j],
                    device_id=(px, py, c), device_id_type=MESH).start()
        token[...] = jnp.zeros_like(token)

    lands = [_hbm(lax.empty(a.shape, a.dtype)) for a in arrs]
    out = pl.pallas_call(
        body, name=name,
        out_shape=(*[pltpu.SemaphoreType.DMA(())] * (2 * n_sem),
                   *[pltpu.HBM(a.shape, a.dtype) for a in arrs], *[pltpu.HBM(a.shape, a.dtype) for a in arrs],
                   jax.ShapeDtypeStruct((8, 128), F32)),
        in_specs=[HBM_SPEC] * (2 * n) + [ANY_SPEC] * len(after),
        out_specs=(*[SEM_SPEC] * (2 * n_sem), *[HBM_SPEC] * (2 * n), VMEM_SPEC),
        input_output_aliases={i: 2 * n_sem + i for i in range(2 * n)},
        compiler_params=pltpu.CompilerParams(has_side_effects=EFFECT),
    )(*[_hbm(a) for a in arrs], *lands, *after)
    base = 2 * n_sem
    return list(out[:base]), list(out[base:base + n]), list(out[base + n:base + 2 * n]), out[-1]


def _scatter_wait(sems, arrs, lands, after, name):
    n = len(arrs)
    n_sem = 3 * n

    def body(*refs):
        ins, lnd = refs[:n], refs[n:2 * n]
        s_sems, r_sems = refs[2 * n:2 * n + n_sem], refs[2 * n + n_sem:2 * n + 2 * n_sem]
        x, y, c, chips = _place()
        for k in range(n):
            for j, (px, py) in enumerate(chips):
                cp = pltpu.make_async_remote_copy(
                    src_ref=ins[k].at[2 * px + py], dst_ref=lnd[k].at[2 * px + py],
                    send_sem=s_sems[3 * k + j], recv_sem=r_sems[3 * k + j],
                    device_id=(px, py, c), device_id_type=MESH)
                cp.wait_send()
                cp.wait_recv()

    out = pl.pallas_call(
        body, name=name,
        out_shape=tuple(pltpu.HBM(a.shape, a.dtype) for a in list(arrs) + list(lands)),
        in_specs=[HBM_SPEC] * (2 * n) + [SEM_SPEC] * (2 * n_sem) + [ANY_SPEC] * len(after),
        out_specs=[HBM_SPEC] * (2 * n),
        input_output_aliases={i: i for i in range(2 * n)},
        compiler_params=pltpu.CompilerParams(has_side_effects=EFFECT),
    )(*arrs, *lands, *sems, *after)
    return list(out[:n]), list(out[n:])


def _sum_owner(chip_arr, pairs, got, name):
    nb, h, cols = got.shape
    tr = _row_tile(h, 16, 512)

    def body(chip_ref, own_ref, a_ref, b_ref, c_ref, o_ref):
        o_ref[...] = ((own_ref[0].astype(F32) + a_ref[0].astype(F32)) + b_ref[0].astype(F32)) + c_ref[0].astype(F32)

    def slot(off):
        return pl.BlockSpec((1, tr, cols), lambda i, chip_ref: ((chip_ref[0] + off) % N_CHIP, i, 0))

    return pl.pallas_call(
        body, name=name,
        grid_spec=pltpu.PrefetchScalarGridSpec(
            num_scalar_prefetch=1, grid=(h // tr,),
            in_specs=[slot(0), slot(1), slot(2), slot(3)],
            out_specs=pl.BlockSpec((tr, cols), lambda i, chip_ref: (i, 0))),
        out_shape=jax.ShapeDtypeStruct((h, cols), F32),
        compiler_params=_cparams(dimension_semantics=("parallel",)),
    )(chip_arr, pairs, got, got, got)


def _swap_start(arrs, name, after=()):
    n = len(arrs)
    first = 2 * n + len(after)

    def body(*refs):
        ins, lnd = refs[:n], refs[n:2 * n]
        send_sems, recv_sems = refs[first:first + n], refs[first + n:first + 2 * n]
        x, y, c, _ = _place()
        for k in range(n):
            pltpu.make_async_remote_copy(
                src_ref=ins[k], dst_ref=lnd[k], send_sem=send_sems[k], recv_sem=recv_sems[k],
                device_id=(x, y, 1 - c), device_id_type=MESH).start()

    lands = [_hbm(lax.empty(a.shape, a.dtype)) for a in arrs]
    out = pl.pallas_call(
        body, name=name,
        out_shape=(*[pltpu.SemaphoreType.DMA(())] * (2 * n), *[pltpu.HBM(a.shape, a.dtype) for a in arrs],
                   *[pltpu.HBM(a.shape, a.dtype) for a in arrs]),
        in_specs=[HBM_SPEC] * (2 * n) + [ANY_SPEC] * len(after),
        out_specs=(*[SEM_SPEC] * (2 * n), *[HBM_SPEC] * (2 * n)),
        input_output_aliases={i: 2 * n + i for i in range(2 * n)},
        compiler_params=pltpu.CompilerParams(has_side_effects=EFFECT),
    )(*[_hbm(a) for a in arrs], *lands, *after)
    return list(out[:2 * n]), list(out[2 * n:3 * n]), list(out[3 * n:4 * n])


def _swap_wait(sems, arrs, lands, after, name):
    n = len(arrs)

    def body(*refs):
        ins, lnd = refs[:n], refs[n:2 * n]
        s_sems, r_sems = refs[2 * n:3 * n], refs[3 * n:4 * n]
        x, y, c, _ = _place()
        for k in range(n):
            cp = pltpu.make_async_remote_copy(
                src_ref=ins[k], dst_ref=lnd[k], send_sem=s_sems[k], recv_sem=r_sems[k],
                device_id=(x, y, 1 - c), device_id_type=MESH)
            cp.wait_send()
            cp.wait_recv()

    out = pl.pallas_call(
        body, name=name,
        out_shape=tuple(pltpu.HBM(a.shape, a.dtype) for a in list(arrs) + list(lands)),
        in_specs=[HBM_SPEC] * (2 * n) + [SEM_SPEC] * (2 * n) + [ANY_SPEC] * len(after),
        out_specs=[HBM_SPEC] * (2 * n),
        input_output_aliases={i: i for i in range(2 * n)},
        compiler_params=pltpu.CompilerParams(has_side_effects=EFFECT),
    )(*arrs, *lands, *sems, *after)
    return list(out[:n]), list(out[n:])


def _row_tile(h, mult=8, cap=512):
    for t in range(cap - cap % mult, mult - 1, -mult):
        if h % t == 0:
            return t
    if mult > 8:
        return _row_tile(h, 8, cap)
    raise ValueError(h)


def _pair_sum(c_arr, full, recv, name):
    nb, rows, cols = full.shape

    def body(c_ref, f_ref, r_ref, o_ref):
        o_ref[...] = (f_ref[...] + r_ref[...]).astype(BF16)

    if _halved_by_rows(full.shape):
        h = rows // 2
        tr = _row_tile(h, 16, 512)
        steps = h // tr
        own = pl.BlockSpec((1, tr, cols), lambda b, i, c_ref: (b, c_ref[0] * steps + i, 0))
        half = pl.BlockSpec((1, tr, cols), lambda b, i, c_ref: (b, i, 0))
    else:
        steps = 1
        own = pl.BlockSpec((1, rows, cols // 2), lambda b, i, c_ref: (b, 0, c_ref[0]))
        half = pl.BlockSpec((1, rows, cols // 2), lambda b, i, c_ref: (b, 0, 0))
    return pl.pallas_call(
        body, name=name,
        grid_spec=pltpu.PrefetchScalarGridSpec(
            num_scalar_prefetch=1, grid=(nb, steps), in_specs=[own, half], out_specs=half),
        out_shape=jax.ShapeDtypeStruct(_half_shape(full.shape), BF16),
        compiler_params=_cparams(dimension_semantics=("parallel", "parallel")),
    )(c_arr, full, recv)


def _adam_math(g, w, m, v):
    m1 = ADAM_B1 * m + (1.0 - ADAM_B1) * g
    v1 = ADAM_B2 * v + (1.0 - ADAM_B2) * (g * g)
    m_hat = m1 / (1.0 - ADAM_B1 ** ADAM_STEP)
    v_hat = v1 / (1.0 - ADAM_B2 ** ADAM_STEP)
    delta = -ADAM_LR * (m_hat / (jnp.sqrt(v_hat) + ADAM_EPS) + ADAM_WD * w)
    return delta, m1, v1


def _adamw_halves(c_arr, own, other, w, m, v, name):
    rows, cols = w.shape
    by_rows = own.shape[1] == cols

    def body(c_ref, own_ref, oth_ref, w_ref, m_ref, v_ref, g_out, d_out, m_out, v_out):
        if by_rows:
            g = jnp.where(pl.program_id(0) == c_ref[0], own_ref[...], oth_ref[...])
        else:
            own_, oth_ = own_ref[...], oth_ref[...]
            g = jnp.where(c_ref[0] == 0, jnp.concatenate([own_, oth_], axis=1), jnp.concatenate([oth_, own_], axis=1))
        d, m1, v1 = _adam_math(g, w_ref[...], m_ref[...], v_ref[...])
        g_out[...] = g
        d_out[...] = d
        m_out[...] = m1
        v_out[...] = v1

    if by_rows:
        h = rows // 2
        tr = _row_tile(h)
        steps = h // tr
        grid = (2, steps)
        half_spec = pl.BlockSpec((tr, cols), lambda p, i, c_ref: (i, 0))
        full_spec = pl.BlockSpec((tr, cols), lambda p, i, c_ref: (p * steps + i, 0))
    else:
        tr = _row_tile(rows)
        grid = (1, rows // tr)
        half_spec = pl.BlockSpec((tr, cols // 2), lambda p, i, c_ref: (i, 0))
        full_spec = pl.BlockSpec((tr, cols), lambda p, i, c_ref: (i, 0))
    return pl.pallas_call(
        body, name=name,
        grid_spec=pltpu.PrefetchScalarGridSpec(
            num_scalar_prefetch=1, grid=grid,
            in_specs=[half_spec, half_spec, full_spec, full_spec, full_spec],
            out_specs=[full_spec] * 4),
        out_shape=[jax.ShapeDtypeStruct(w.shape, F32)] * 4,
        compiler_params=_cparams(dimension_semantics=("parallel", "parallel")),
    )(c_arr, own, other, w, m, v)


def _adamw_whole(items, name):
    n = len(items)

    def body(*refs):
        ins, outs = refs[:4 * n], refs[4 * n:]
        for k in range(n):
            g, w, m, v = (r[...] for r in ins[4 * k:4 * k + 4])
            d, m1, v1 = _adam_math(g, w, m, v)
            outs[3 * k][...] = d
            outs[3 * k + 1][...] = m1
            outs[3 * k + 2][...] = v1

    flat = [a for it in items for a in it]
    shapes = [jax.ShapeDtypeStruct(it[1].shape, F32) for it in items for _ in range(3)]
    out = pl.pallas_call(
        body, name=name, out_shape=shapes,
        in_specs=[VMEM_SPEC] * (4 * n), out_specs=[VMEM_SPEC] * (3 * n),
        compiler_params=_cparams(),
    )(*flat)
    return [tuple(out[3 * k:3 * k + 3]) for k in range(n)]


def _adamw_tiled(g, w, m, v, name):
    rows, cols = w.shape
    tr = _row_tile(rows)

    def body(g_ref, w_ref, m_ref, v_ref, d_out, m_out, v_out):
        d, m1, v1 = _adam_math(g_ref[...], w_ref[...], m_ref[...], v_ref[...])
        d_out[...] = d
        m_out[...] = m1
        v_out[...] = v1

    spec = pl.BlockSpec((tr, cols), lambda i: (i, 0))
    return pl.pallas_call(
        body, name=name, grid=(rows // tr,),
        out_shape=[jax.ShapeDtypeStruct(w.shape, F32)] * 3,
        in_specs=[spec] * 4, out_specs=[spec] * 3,
        compiler_params=_cparams(dimension_semantics=("parallel",)),
    )(g, w, m, v)


def _mod_forward(cond, w_mod, b_mod_cols, name):
    def body(c_ref, w_ref, b_ref, o_ref):
        o_ref[...] = _dot(_silu(c_ref[...]), w_ref[...]) + b_ref[...]

    return pl.pallas_call(
        body, name=name, out_shape=jax.ShapeDtypeStruct((cond.shape[0], w_mod.shape[1]), F32),
        in_specs=[VMEM_SPEC] * 3, out_specs=VMEM_SPEC, compiler_params=_cparams(),
    )(cond, w_mod, b_mod_cols)


def _mod_backward(cond, w_mod, dmod_cols, name):
    def body(c_ref, w_ref, d_ref, gw_ref, gc_ref):
        s = _silu(c_ref[...])
        d = d_ref[...]
        gw_ref[...] = _dot_tn(s, d)
        gc_ref[...] = _dot_nt(d[8:16, :], w_ref[...])

    return pl.pallas_call(
        body, name=name,
        out_shape=[jax.ShapeDtypeStruct(w_mod.shape, F32), jax.ShapeDtypeStruct((8, w_mod.shape[0]), F32)],
        in_specs=[VMEM_SPEC] * 3, out_specs=[VMEM_SPEC] * 2, compiler_params=_cparams(),
    )(cond, w_mod, dmod_cols)


def _col_chunks(width, step=512):
    return [(s, min(step, width - s)) for s in range(0, width, step)]


def _w_in_row(p_off):
    if p_off < 9 * HW:
        return p_off
    return 9 * HW if p_off == OFF_LR else p_off + 2 * RANK


def _in_projection(ctx0, x0, modc, modx, pre1, w_t, n_ctx_tiles, name):
    d = x0.shape[1]
    rows = ctx0.shape[0] + x0.shape[0]
    width = P_WIDTH

    half = 9 * HW

    def body(ctx_ref, x_ref, modc_ref, modx_ref, pre_ref, w_hbm, h_ref, p_ref, w_ref, sems):
        first = pl.program_id(0) == 0
        copies = [pltpu.make_async_copy(w_hbm.at[0:half, :], w_ref.at[0:half, :], sems.at[0]),
                  pltpu.make_async_copy(w_hbm.at[half:, :], w_ref.at[half:, :], sems.at[1])]

        @pl.when(first)
        def _():
            copies[0].start()
            copies[1].start()

        is_ctx = pl.program_id(0) < n_ctx_tiles
        n, _ = _rms(jnp.where(is_ctx, ctx_ref[...], x_ref[...]))
        shift = jnp.where(is_ctx, modc_ref[0:1, :], modx_ref[0:1, :])
        scale = jnp.where(is_ctx, modc_ref[1:2, :], modx_ref[1:2, :])
        h = (n * pre_ref[...] * (1.0 + scale) + shift).astype(BF16)
        h_ref[...] = h
        for part in range(2):
            pl.when(first)(copies[part].wait)
            for s, w in _col_chunks(width):
                if (s < half) == (part == 0):
                    p_ref[:, s:s + w] = _dot_nt(h, w_ref[_w_in_row(s):_w_in_row(s) + w, :])

    row = lambda i: (i, 0)
    fixed = lambda i: (0, 0)
    return pl.pallas_call(
        body, name=name, grid=(rows // TM,),
        out_shape=[jax.ShapeDtypeStruct((rows, d), BF16), jax.ShapeDtypeStruct((rows, width), F32)],
        in_specs=[pl.BlockSpec((TM, d), lambda i: (jnp.minimum(i, n_ctx_tiles - 1), 0)),
                  pl.BlockSpec((TM, d), lambda i: (jnp.maximum(i - n_ctx_tiles, 0), 0)),
                  pl.BlockSpec((8, d), fixed), pl.BlockSpec((8, d), fixed), pl.BlockSpec((1, d), fixed), ANY_SPEC],
        out_specs=[pl.BlockSpec((TM, d), row), pl.BlockSpec((TM, width), row)],
        scratch_shapes=[pltpu.VMEM(w_t.shape, w_t.dtype), pltpu.SemaphoreType.DMA((2,))],
        compiler_params=_cparams(dimension_semantics=("arbitrary",)),
    )(ctx0, x0, modc, modx, pre1, w_t)


C_HQ, C_HI, C_HF_FW, C_HF_BW, C_HGATE, C_GQ, C_GK, C_GV, C_GGATE = range(9)
OFF_GATE_HG = 9 * HW
OFF_LR = 13 * HW
P_WIDTH = OFF_LR + 128


def _head_norm_fwd(o, w):
    outs, ns, rs = [], [], []
    for h in range(NH):
        n, r = _rms(o[:, h * HD:(h + 1) * HD])
        ns.append(n)
        rs.append(r)
        outs.append(n * w)
    return jnp.concatenate(outs, axis=1), ns, rs


def _mixer_tail(z, o_hg, o_gla, p_hgate, p_ggate, p_gate_hg, p_gate_gla, hg_on, gla_on, wbh, wbg, wout):
    on_hg, n_hg, r_hg = _head_norm_fwd(o_hg, hg_on)
    on_gla, n_gla, r_gla = _head_norm_fwd(o_gla, gla_on)
    og_hg = (on_hg * _silu(p_hgate)).astype(BF16)
    og_gla = (on_gla * _silu(p_ggate)).astype(BF16)
    b_hg = jnp.dot(og_hg, wbh, preferred_element_type=F32)
    b_gla = jnp.dot(og_gla, wbg, preferred_element_type=F32)
    s_hg = _sigmoid(p_gate_hg)
    s_gla = _sigmoid(p_gate_gla)
    merged = (s_hg * b_hg + s_gla * b_gla).astype(BF16)
    y1 = jnp.dot(merged, wout, preferred_element_type=F32)
    return dict(on_hg=on_hg, n_hg=n_hg, r_hg=r_hg, on_gla=on_gla, n_gla=n_gla, r_gla=r_gla, og_hg=og_hg,
                og_gla=og_gla, b_hg=b_hg, b_gla=b_gla, s_hg=s_hg, s_gla=s_gla, merged=merged, y1=y1)


def _mixer_tail_fwd(x_lat, p, o_list, modx, norms, onorms, w_br_hg, w_br_gla, w_out, n_ctx_tiles, name):
    rows, d = x_lat.shape

    def body(x_ref, ofw_hg, obw_hg, ofw_gla, obw_gla, p_hgate, p_ggate, p_ghg_a, p_ghg_b, p_ggla_a, p_ggla_b,
             modx_ref, norm_ref, on_ref, wbh_ref, wbg_ref, wout_ref, z2_ref, y1_ref, mrg_ref, oghg_ref, oggla_ref):
        p_gate_hg = jnp.concatenate([p_ghg_a[...], p_ghg_b[...]], axis=1)
        p_gate_gla = jnp.concatenate([p_ggla_a[...], p_ggla_b[...]], axis=1)
        t = _mixer_tail(x_ref[...], ofw_hg[...] + obw_hg[...], ofw_gla[...] + obw_gla[...], p_hgate[...],
                        p_ggate[...], p_gate_hg, p_gate_gla, on_ref[0:1, 0:HD], on_ref[1:2, 0:HD],
                        wbh_ref[...], wbg_ref[...], wout_ref[...])
        y1_ref[...] = t["y1"]
        mrg_ref[...] = t["merged"]
        oghg_ref[...] = t["og_hg"]
        oggla_ref[...] = t["og_gla"]
        n1, _ = _rms(t["y1"])
        z2_ref[...] = x_ref[...] + n1 * norm_ref[1:2, :] * modx_ref[2:3, :]

    lat = lambda i: (i, 0)
    full = lambda i: (i + n_ctx_tiles, 0)
    fixed = lambda i: (0, 0)

    def pcol(blk):
        return pl.BlockSpec((TM, HW), lambda i: (i + n_ctx_tiles, blk))

    in_specs = ([pl.BlockSpec((TM, d), lat)] + [pl.BlockSpec((TM, HW), full)] * 4
                + [pcol(C_HGATE), pcol(C_GGATE), pcol(9), pcol(10), pcol(11), pcol(12)]
                + [pl.BlockSpec((8, d), fixed)] * 3 + [VMEM_SPEC] * 3)
    bf = lambda w: jax.ShapeDtypeStruct((rows, w), BF16)
    f32 = jax.ShapeDtypeStruct((rows, d), F32)
    return pl.pallas_call(
        body, name=name, grid=(rows // TM,), out_shape=[f32, f32, bf(d), bf(HW), bf(HW)], in_specs=in_specs,
        out_specs=[pl.BlockSpec((TM, d), lat)] * 3 + [pl.BlockSpec((TM, HW), lat)] * 2,
        compiler_params=_cparams(dimension_semantics=("parallel",)),
    )(x_lat, *o_list, p, p, p, p, p, p, modx, norms, onorms, w_br_hg, w_br_gla, w_out)


def _ffn_fwd_bwd(z2, modx, norms, w_gate, w_up, w_down, target, name):
    rows, d = z2.shape
    dff = w_gate.shape[0]
    inv_d = 1.0 / d

    def body(z2_ref, modx_ref, norm_ref, wg_ref, wu_ref, wd_ref, t_ref,
             loss_ref, dz2_ref, h2_ref, a_ref, du_ref, dv_ref, dy2_ref, stat_ref):
        i = pl.program_id(0)
        pre2, post2 = norm_ref[2:3, :], norm_ref[3:4, :]
        shift2, scale2, gate2 = modx_ref[3:4, :], modx_ref[4:5, :], modx_ref[5:6, :]
        z2 = z2_ref[...]
        n2, r2 = _rms(z2)
        nw2 = n2 * pre2
        h2 = (nw2 * (1.0 + scale2) + shift2).astype(BF16)
        h2_ref[...] = h2
        u = _dot_nt(h2, wg_ref[...])
        v = _dot_nt(h2, wu_ref[...])
        su = _silu(u)
        a = (su * v).astype(BF16)
        a_ref[...] = a
        y2 = jnp.dot(a, wd_ref[...], preferred_element_type=F32)
        n3, r3 = _rms(y2)
        err = z2 + n3 * post2 * gate2 - t_ref[...]
        part = 0.5 * inv_d * jnp.sum(err * err)
        dz3 = err * inv_d
        dgate2 = _colsum(dz3 * n3 * post2)
        tt = dz3 * gate2
        dpost2 = _colsum(tt * n3)
        dy2 = _rms_bwd(tt * post2, n3, r3).astype(BF16)
        dy2_ref[...] = dy2
        da = _dot_nt(dy2, wd_ref[...])
        du = (da * v * _dsilu(u)).astype(BF16)
        dv = (da * su).astype(BF16)
        du_ref[...] = du
        dv_ref[...] = dv
        dh2 = (jnp.dot(du, wg_ref[...], preferred_element_type=F32)
               + jnp.dot(dv, wu_ref[...], preferred_element_type=F32))
        dshift2 = _colsum(dh2)
        dscale2 = _colsum(dh2 * nw2)
        dnw2 = dh2 * (1.0 + scale2)
        dpre2 = _colsum(dnw2 * n2)
        dz2_ref[...] = dz3 + _rms_bwd(dnw2 * pre2, n2, r2)

        @pl.when(i == 0)
        def _():
            stat_ref[...] = jnp.zeros_like(stat_ref)
            loss_ref[...] = jnp.zeros_like(loss_ref)

        for r, val in enumerate((dshift2, dscale2, dgate2, dpre2, dpost2)):
            stat_ref[r:r + 1, :] += val
        loss_ref[...] += part
        stat_ref[5:6, 0:128] += part

    lat = lambda i: (i, 0)
    fixed = lambda i: (0, 0)
    bf = lambda w: jax.ShapeDtypeStruct((rows, w), BF16)
    return pl.pallas_call(
        body, name=name, grid=(rows // TM,),
        out_shape=[jax.ShapeDtypeStruct((8, 128), F32), jax.ShapeDtypeStruct((rows, d), F32), bf(d), bf(dff), bf(dff),
                   bf(dff), bf(d), jax.ShapeDtypeStruct((8, d), F32)],
        in_specs=[pl.BlockSpec((TM, d), lat), pl.BlockSpec((8, d), fixed), pl.BlockSpec((8, d), fixed)]
        + [VMEM_SPEC] * 3 + [pl.BlockSpec((TM, d), lat)],
        out_specs=[pl.BlockSpec((8, 128), fixed), pl.BlockSpec((TM, d), lat), pl.BlockSpec((TM, d), lat),
                   pl.BlockSpec((TM, dff), lat), pl.BlockSpec((TM, dff), lat), pl.BlockSpec((TM, dff), lat),
                   pl.BlockSpec((TM, d), lat), pl.BlockSpec((8, d), fixed)],
        compiler_params=_cparams(dimension_semantics=("arbitrary",)),
    )(z2, modx, norms, w_gate, w_up, w_down, target)


def _mixer_tail_bwd(x_lat, p, o_list, dz2, y1, modx, norms, onorms, w_br_hg, w_br_gla, w_out, n_ctx_tiles, n_tiles,
                    name):
    rows, d = x_lat.shape
    total = n_tiles * TM

    def body(x_ref, ofw_hg, obw_hg, ofw_gla, obw_gla, p_hgate, p_ggate, p_ghg_a, p_ghg_b, p_ggla_a, p_ggla_b,
             dz2_ref, y1_ref, modx_ref, norm_ref, on_ref, wbh_ref, wbg_ref, wout_ref,
             dohg_ref, dogla_ref, dhgate_ref, dggate_ref, dghg_ref, dggla_ref, dy1_ref, dbhg_ref, dbgla_ref,
             stat_ref):
        i = pl.program_id(0)

        @pl.when(i == 0)
        def _():
            stat_ref[...] = jnp.zeros_like(stat_ref)

        @pl.when(i < n_ctx_tiles)
        def _():
            for ref in (dohg_ref, dogla_ref, dhgate_ref, dggate_ref, dghg_ref, dggla_ref):
                ref[...] = jnp.zeros_like(ref)

        @pl.when(i >= n_ctx_tiles)
        def _():
            post1, gate1 = norm_ref[1:2, :], modx_ref[2:3, :]
            hg_on, gla_on = on_ref[0:1, 0:HD], on_ref[1:2, 0:HD]
            p_gate_hg = jnp.concatenate([p_ghg_a[...], p_ghg_b[...]], axis=1)
            p_gate_gla = jnp.concatenate([p_ggla_a[...], p_ggla_b[...]], axis=1)
            ph, pg = p_hgate[...], p_ggate[...]
            t = _mixer_tail(x_ref[...], ofw_hg[...] + obw_hg[...], ofw_gla[...] + obw_gla[...], ph, pg,
                            p_gate_hg, p_gate_gla, hg_on, gla_on, wbh_ref[...], wbg_ref[...], wout_ref[...])
            dz2 = dz2_ref[...]
            n1, r1 = _rms(y1_ref[...])
            dgate1 = _colsum(dz2 * n1 * post1)
            tt = dz2 * gate1
            dpost1 = _colsum(tt * n1)
            dy1 = _rms_bwd(tt * post1, n1, r1).astype(BF16)
            dy1_ref[...] = dy1
            dmerged = _dot_nt(dy1, wout_ref[...])
            dghg_ref[...] = (dmerged * t["b_hg"] * t["s_hg"] * (1.0 - t["s_hg"])).astype(BF16)
            dggla_ref[...] = (dmerged * t["b_gla"] * t["s_gla"] * (1.0 - t["s_gla"])).astype(BF16)
            db_hg = (dmerged * t["s_hg"]).astype(BF16)
            db_gla = (dmerged * t["s_gla"]).astype(BF16)
            dbhg_ref[...] = db_hg
            dbgla_ref[...] = db_gla
            don_acc = []
            for (db, wb, pgate, on, ns, rs, gain, gate_ref, do_ref) in (
                    (db_hg, wbh_ref, ph, t["on_hg"], t["n_hg"], t["r_hg"], hg_on, dhgate_ref, dohg_ref),
                    (db_gla, wbg_ref, pg, t["on_gla"], t["n_gla"], t["r_gla"], gla_on, dggate_ref, dogla_ref)):
                dog = _dot_nt(db, wb[...])
                gate_ref[...] = (dog * on * _dsilu(pgate)).astype(BF16)
                don = dog * _silu(pgate)
                acc = jnp.zeros((1, HD), F32)
                for h in range(NH):
                    sl = slice(h * HD, (h + 1) * HD)
                    acc = acc + _colsum(don[:, sl] * ns[h])
                    do_ref[:, sl] = _rms_bwd(don[:, sl] * gain, ns[h], rs[h]).astype(BF16)
                don_acc.append(acc)
            stat_ref[0:1, :] += dgate1
            stat_ref[1:2, :] += dpost1
            stat_ref[2:3, 0:HD] += don_acc[0]
            stat_ref[2:3, HD:2 * HD] += don_acc[1]

    lat = lambda i: (jnp.maximum(i - n_ctx_tiles, 0), 0)
    full = lambda i: (i, 0)
    fixed = lambda i: (0, 0)

    def pcol(blk):
        return pl.BlockSpec((TM, HW), lambda i: (i, blk))

    in_specs = ([pl.BlockSpec((TM, d), lat)] + [pl.BlockSpec((TM, HW), full)] * 4
                + [pcol(C_HGATE), pcol(C_GGATE), pcol(9), pcol(10), pcol(11), pcol(12)]
                + [pl.BlockSpec((TM, d), lat), pl.BlockSpec((TM, d), lat)]
                + [pl.BlockSpec((8, d), fixed)] * 3 + [VMEM_SPEC] * 3)
    f = lambda w: jax.ShapeDtypeStruct((total, w), BF16)
    out_shape = [f(HW), f(HW), f(HW), f(HW), f(d), f(d), jax.ShapeDtypeStruct((rows, d), BF16),
                 jax.ShapeDtypeStruct((rows, d), BF16), jax.ShapeDtypeStruct((rows, d), BF16),
                 jax.ShapeDtypeStruct((8, d), F32)]
    out_specs = ([pl.BlockSpec((TM, HW), full)] * 4 + [pl.BlockSpec((TM, d), full)] * 2
                 + [pl.BlockSpec((TM, d), lat)] * 3 + [pl.BlockSpec((8, d), fixed)])
    return pl.pallas_call(
        body, name=name, grid=(n_tiles,), out_shape=out_shape, in_specs=in_specs, out_specs=out_specs,
        compiler_params=_cparams(dimension_semantics=("arbitrary",)),
    )(x_lat, *o_list, p, p, p, p, p, p, dz2, y1, modx, norms, onorms, w_br_hg, w_br_gla, w_out)


def _in_projection_bwd(ctx0, x0, dz2, modc, modx, pre1, w_t, pieces, n_ctx_tiles, name):
    d = x0.shape[1]
    rows = ctx0.shape[0] + x0.shape[0]
    lat_rows = dz2.shape[0]
    width = P_WIDTH
    n_pieces = len(pieces)

    def body(*refs):
        ctx_ref, x_ref, dz2_ref, modc_ref, modx_ref, pre_ref, w_ref = refs[:7]
        (dhq_f, dhq_b, dhi_f, dhi_b, dhf_f, dhf_b, dhgate, dgq_f, dgq_b, dgk_f, dgk_b, dgv_f, dgv_b, dggate,
         dghg, dggla, dlr_f, dlr_b) = refs[7:7 + n_pieces]
        dp_ref, gx_ref, stat_ref = refs[7 + n_pieces:]
        i = pl.program_id(0)
        is_ctx = i < n_ctx_tiles
        z = jnp.where(is_ctx, ctx_ref[...], x_ref[...])
        sections = [
            (0, dhq_f[...] + dhq_b[...]), (HW, dhi_f[...] + dhi_b[...]), (2 * HW, dhf_f[...]), (3 * HW, dhf_b[...]),
            (4 * HW, dhgate[...]), (5 * HW, dgq_f[...] + dgq_b[...]), (6 * HW, dgk_f[...] + dgk_b[...]),
            (7 * HW, dgv_f[...] + dgv_b[...]), (8 * HW, dggate[...]),
            (9 * HW, dghg[:, 0:HW]), (10 * HW, dghg[:, HW:2 * HW]),
            (11 * HW, dggla[:, 0:HW]), (12 * HW, dggla[:, HW:2 * HW]), (OFF_LR, dlr_f[...] + dlr_b[...])]
        dh = jnp.zeros((TM, d), F32)
        for off, val in sections:
            w = val.shape[1]
            vb = val.astype(BF16)
            dp_ref[off:off + w, :] = vb.T
            dh = dh + jnp.dot(vb, w_ref[_w_in_row(off):_w_in_row(off) + w, :], preferred_element_type=F32)
        n, r = _rms(z)
        pre = pre_ref[...]
        scale = jnp.where(is_ctx, modc_ref[1:2, :], modx_ref[1:2, :])
        nw = n * pre
        dshift = _colsum(dh)
        dscale = _colsum(dh * nw)
        dnw = dh * (1.0 + scale)
        dpre = _colsum(dnw * n)
        gx_ref[...] = dz2_ref[...] + _rms_bwd(dnw * pre, n, r)
        zero = jnp.zeros((1, d), F32)

        @pl.when(i == 0)
        def _():
            stat_ref[...] = jnp.zeros_like(stat_ref)

        stat_ref[0:1, :] += jnp.where(is_ctx, zero, dshift)
        stat_ref[1:2, :] += jnp.where(is_ctx, zero, dscale)
        stat_ref[2:3, :] += jnp.where(is_ctx, dshift, zero)
        stat_ref[3:4, :] += jnp.where(is_ctx, dscale, zero)
        stat_ref[4:5, :] += dpre

    full = lambda i: (i, 0)
    lat = lambda i: (jnp.maximum(i - n_ctx_tiles, 0), 0)
    fixed = lambda i: (0, 0)
    piece_specs = [pl.BlockSpec((TM, a.shape[1]), full) for a in pieces]
    in_specs = [pl.BlockSpec((TM, d), lambda i: (jnp.minimum(i, n_ctx_tiles - 1), 0)), pl.BlockSpec((TM, d), lat),
                pl.BlockSpec((TM, d), lat), pl.BlockSpec((8, d), fixed),
                pl.BlockSpec((8, d), fixed), pl.BlockSpec((1, d), fixed), VMEM_SPEC] + piece_specs
    return pl.pallas_call(
        body, name=name, grid=(rows // TM,),
        out_shape=[jax.ShapeDtypeStruct((width, rows), BF16), jax.ShapeDtypeStruct((lat_rows, d), F32),
                   jax.ShapeDtypeStruct((8, d), F32)],
        in_specs=in_specs,
        out_specs=[pl.BlockSpec((width, TM), lambda i: (0, i)), pl.BlockSpec((TM, d), lat),
                   pl.BlockSpec((8, d), fixed)],
        compiler_params=_cparams(dimension_semantics=("arbitrary",)),
    )(ctx0, x0, dz2, modc, modx, pre1, w_t, *pieces)


def _transposed_lhs_matmul(x_ref, dy_ref, o_ref, xt_ref):
    @pl.when(pl.program_id(1) == 0)
    def _():
        xt_ref[...] = x_ref[...].T

    o_ref[...] = jnp.dot(xt_ref[...], dy_ref[...], preferred_element_type=F32)


def _w_in_grad(dp_t, h1, n_cols, name, after=()):
    rows, d = h1.shape
    n_main = OFF_LR // HW
    lr0 = _w_in_row(OFF_LR)

    def body(x_ref, xlr_ref, h_ref, *rest):
        o_hbm, acc_ref, sems = rest[len(after):]
        i = pl.program_id(0)
        slot = i % 2

        def main_copy(step):
            row = jnp.where(step < 9, step * HW, step * HW + 2 * RANK)
            return pltpu.make_async_copy(acc_ref.at[step % 2], o_hbm.at[pl.ds(pl.multiple_of(row, 8), HW), :],
                                         sems.at[step % 2])

        @pl.when(i > 1)
        def _():
            main_copy(i - 2).wait()

        @pl.when(i < n_main)
        def _():
            acc_ref[slot] = jnp.dot(x_ref[...], h_ref[...], preferred_element_type=F32)
            main_copy(i).start()

        @pl.when(i == n_main)
        def _():
            acc_ref[slot, 0:128, :] = jnp.dot(xlr_ref[...], h_ref[...], preferred_element_type=F32)
            lr_copy = pltpu.make_async_copy(acc_ref.at[slot, 0:2 * RANK, :], o_hbm.at[lr0:lr0 + 2 * RANK, :],
                                            sems.at[slot])
            lr_copy.start()
            main_copy(i - 1).wait()
            lr_copy.wait()

    return pl.pallas_call(
        body, name=name, grid=(n_main + 1,),
        out_shape=jax.ShapeDtypeStruct((n_cols, d), F32),
        in_specs=[pl.BlockSpec((HW, rows), lambda i: (jnp.minimum(i, n_main - 1), 0)),
                  pl.BlockSpec((128, rows), lambda i: (OFF_LR // 128, 0)),
                  pl.BlockSpec((rows, d), lambda i: (0, 0))] + [ANY_SPEC] * len(after),
        out_specs=ANY_SPEC,
        scratch_shapes=[pltpu.VMEM((2, HW, d), F32), pltpu.SemaphoreType.DMA((2,))],
        compiler_params=_cparams(dimension_semantics=("arbitrary",)),
    )(dp_t, dp_t, h1, *after)


def _weight_grad(xs, dy, name, tk=None, tn=512, k_first=0, k_tiles=None):
    rows = dy.shape[0]
    n = dy.shape[1]
    tn_ = min(tn, n)
    tk_ = xs.shape[1] if tk is None else tk
    k_tiles = xs.shape[1] // tk_ if k_tiles is None else k_tiles
    k = k_tiles * tk_

    return pl.pallas_call(
        functools.partial(_transposed_lhs_matmul), name=name, grid=(k_tiles, n // tn_),
        out_shape=jax.ShapeDtypeStruct((k, n), F32),
        in_specs=[pl.BlockSpec((rows, tk_), lambda i, j: (0, i + k_first)),
                  pl.BlockSpec((rows, tn_), lambda i, j: (0, j))],
        out_specs=pl.BlockSpec((tk_, tn_), lambda i, j: (i, j)),
        scratch_shapes=[pltpu.VMEM((tk_, rows), BF16)],
        compiler_params=_cparams(dimension_semantics=("parallel", "arbitrary")),
    )(xs, dy)


def _running_sums(xs, fws):
    c = xs[0].shape[0]
    row = lax.broadcasted_iota(jnp.int32, (c, 1), 0)
    s = 1
    while s < c:
        xs = [x + (jnp.where(row >= s, pltpu.roll(x, s, axis=0), 0.0) if fw else
                   jnp.where(row < c - s, pltpu.roll(x, c - s, axis=0), 0.0)) for x, fw in zip(xs, fws)]
        s *= 2
    return xs


def _chunks_terms(qs, ks, gs, fws):
    c = CHUNK
    n = len(qs)
    r = lax.broadcasted_iota(jnp.int32, (c, c), 0)
    s = lax.broadcasted_iota(jnp.int32, (c, c), 1)
    row = lax.broadcasted_iota(jnp.int32, (c, 1), 0)
    per_dir = {}
    for fw in set(fws):
        pos = row if fw else (c - 1 - row)
        per_dir[fw] = dict(
            causal=(s <= r) if fw else (s >= r), causal_t=(s >= r) if fw else (s <= r), pos=pos,
            in_blk=[(pos >= SUB * j) & (pos < SUB * (j + 1)) for j in range(NSUB)],
            start_row=[None] + [SUB * j - 1 if fw else c - SUB * j for j in range(1, NSUB)],
            rend=c - 1 if fw else 0)
    dirs = [per_dir[fw] for fw in fws]
    cums = _running_sums(gs, fws)
    starts = [[None] + [cum[d["start_row"][j]:d["start_row"][j] + 1, :] for j in range(1, NSUB)]
              for cum, d in zip(cums, dirs)]
    es = [[jnp.exp(cum) for cum in cums]]
    for j in range(1, NSUB):
        es.append([jnp.exp(jnp.where(d["pos"] >= SUB * j, cum - st[j], -1e30)) for cum, st, d in zip(cums, starts, dirs)])
    owns = [functools.reduce(lambda rest, j: jnp.where(d["in_blk"][j], st[j], rest), range(1, NSUB), 0.0)
            for st, d in zip(starts, dirs)]
    kscales = [jnp.exp(own - cum) for own, cum in zip(owns, cums)]
    cends = [cum[d["rend"]:d["rend"] + 1, :] for cum, d in zip(cums, dirs)]
    tails = [jnp.exp(cend - cum) for cend, cum in zip(cends, cums)]
    qcats = [jnp.concatenate([q * es[j][i] for j in range(NSUB)], axis=1).astype(BF16) for i, q in enumerate(qs)]
    kts = [k * ksc for k, ksc in zip(ks, kscales)]
    kms = [jnp.concatenate([jnp.where(d["in_blk"][j], kt, 0.0) for j in range(NSUB)], axis=1).astype(BF16)
           for kt, d in zip(kts, dirs)]
    e_by_lane = [[es[j][i] for j in range(NSUB)] for i in range(n)]
    return dict(dirs=dirs, e=e_by_lane, kscale=kscales, cend=cends, tail=tails, qcat=qcats, km=kms, kt=kts)


def _chunks_fwd(qs, ks, vs, gs, st0s, fws):
    t = _chunks_terms(qs, ks, gs, fws)
    scores = [_dot_nt(qc, km) for qc, km in zip(t["qcat"], t["km"])]
    a = [jnp.where(d["causal"], sc, 0.0) for sc, d in zip(scores, t["dirs"])]
    inter = [_dot_nt(qc[:, 0:HD], st0) for qc, st0 in zip(t["qcat"], st0s)]
    intra = [_dot(a_, v) for a_, v in zip(a, vs)]
    os_ = [x + y for x, y in zip(intra, inter)]
    upd = [_dot_tn(v, k * tl) for v, k, tl in zip(vs, ks, t["tail"])]
    st1s = [st0 * jnp.exp(ce) + u for st0, ce, u in zip(st0s, t["cend"], upd)]
    return os_, st1s


def _chunks_bwd(qs, ks, vs, gs, st0s, dos, dst1s, fws):
    n = len(qs)
    t = _chunks_terms(qs, ks, gs, fws)
    qcat, km, e, dirs = t["qcat"], t["km"], t["e"], t["dirs"]
    a_t = [jnp.where(d["causal_t"], _dot_nt(km_, qc), 0.0) for km_, qc, d in zip(km, qcat, dirs)]
    ktail = [k * tl for k, tl in zip(ks, t["tail"])]
    dv_a = [_dot(at, do) for at, do in zip(a_t, dos)]
    dv_b = [_dot_nt(kt, ds) for kt, ds in zip(ktail, dst1s)]
    dv = [x + y for x, y in zip(dv_a, dv_b)]
    da = [jnp.where(d["causal"], _dot_nt(do, v), 0.0) for do, v, d in zip(dos, vs, dirs)]
    da_t = [jnp.where(d["causal_t"], _dot_nt(v, do), 0.0) for do, v, d in zip(dos, vs, dirs)]
    dqcat = [_dot(da_, km_) for da_, km_ in zip(da, km)]
    dq_inter = [e[i][0] * _dot(dos[i], st0s[i]) for i in range(n)]
    dkm = [_dot(dat, qc) for dat, qc in zip(da_t, qcat)]
    dk_inter = [_dot(v, ds) * tl for v, ds, tl in zip(vs, dst1s, t["tail"])]
    dq = [dq_inter[i] + sum(e[i][j] * dqcat[i][:, j * HD:(j + 1) * HD] for j in range(NSUB)) for i in range(n)]
    dkt = [sum(jnp.where(dirs[i]["in_blk"][j], dkm[i][:, j * HD:(j + 1) * HD], 0.0) for j in range(NSUB))
           for i in range(n)]
    dk = [dkt[i] * t["kscale"][i] + dk_inter[i] for i in range(n)]
    dcum = [qs[i] * dq_inter[i] - ks[i] * dk_inter[i] - t["kt"][i].astype(BF16).astype(F32) * dkt[i]
            + sum(qcat[i][:, j * HD:(j + 1) * HD].astype(F32) * dqcat[i][:, j * HD:(j + 1) * HD] for j in range(NSUB))
            for i in range(n)]
    ecend = [jnp.exp(ce) for ce in t["cend"]]
    end = [ecend[i] * _colsum(st0s[i] * dst1s[i]) + _colsum(ks[i] * dk_inter[i]) for i in range(n)]
    sums = _running_sums(dcum, [not fw for fw in fws])
    dg = [sm + en for sm, en in zip(sums, end)]
    upd = [_dot_tn(dos[i], qs[i] * e[i][0]) for i in range(n)]
    dst0 = [dst1s[i] * ecend[i] + upd[i] for i in range(n)]
    return dq, dk, dv, dg, dst0


def _chunk_index(step, n_ctx_chunks, n_chunks, fw):
    if fw:
        return step
    return jnp.where(step < n_ctx_chunks, n_ctx_chunks - 1 - step, n_chunks - 1 + n_ctx_chunks - step)


def _hg_inputs(hq, hf, lbv, d_idx, sl):
    lb = _sigmoid(lbv[d_idx:d_idx + 1, sl] - lbv[2 + d_idx:3 + d_idx, sl])
    sg = _sigmoid(hf)
    f = lb + (1.0 - lb) * sg
    return _silu(hq), 1.0 - f, jnp.log(f), f, sg, lb


def _scan_fwd_both(p, branch_sides, n_ctx_chunks, name):
    rows = p.shape[0]
    n_chunks = rows // CHUNK
    n_ins = [4 if branch == "hg" else 6 for branch, _ in branch_sides]
    n_in_all = 2 * sum(n_ins)
    n_br = len(branch_sides)

    def body(*refs):
        ins, outs, state = refs[:n_in_all], refs[n_in_all:n_in_all + 4 * n_br], refs[-1]

        @pl.when(pl.program_id(0) == 0)
        def _():
            state[...] = jnp.zeros_like(state)

        lanes, where = [], []
        pos = 0
        for bi, (branch, _) in enumerate(branch_sides):
            hg = branch == "hg"
            n_in = n_ins[bi]
            for di, fw in enumerate((True, False)):
                r = ins[pos:pos + n_in]
                pos += n_in
                o_ref, st_ref = outs[4 * bi + 2 * di], outs[4 * bi + 2 * di + 1]
                if hg:
                    a_ref, b_ref, c_ref, lb_ref = r
                else:
                    a_ref, b_ref, c_ref, lr_ref, wgk_ref, bgk_ref = r
                    logits = _dot(lr_ref[...], wgk_ref[...]) + bgk_ref[...]
                    g_all = _log_sigmoid(logits) * (1.0 / GATE_NORM)
                for h in range(NH):
                    sl = slice(h * HD, (h + 1) * HD)
                    if hg:
                        q, k, g, _, _, _ = _hg_inputs(a_ref[:, sl], c_ref[:, sl], lb_ref[...], di, sl)
                        v = b_ref[:, sl]
                    else:
                        q, k, v, g = a_ref[:, sl] * (HD ** -0.5), b_ref[:, sl], c_ref[:, sl], g_all[:, sl]
                    lanes.append((q, k, v, g, state[2 * bi + di, h], fw))
                    where.append((2 * bi + di, h, sl, o_ref, st_ref))
        qs, ks, vs, gs, st0s, fws = (list(col) for col in zip(*lanes))
        os_, st1s = _chunks_fwd(qs, ks, vs, gs, st0s, fws)
        for (si, h, sl, o_ref, st_ref), st0, o, st1 in zip(where, st0s, os_, st1s):
            st_ref[0, h] = st0
            o_ref[:, sl] = o
            state[si, h] = st1

    fixed = lambda j: (0, 0)
    in_specs, args, out_specs = [], [], []
    for branch, side in branch_sides:
        for di, fw in enumerate((True, False)):
            chunk = functools.partial(_chunk_index, n_ctx_chunks=n_ctx_chunks, n_chunks=n_chunks, fw=fw)

            def cmap(blk, width=HW, chunk=chunk):
                return pl.BlockSpec((CHUNK, width), lambda j: (chunk(j), blk))

            if branch == "hg":
                in_specs += [cmap(C_HQ), cmap(C_HI), cmap(C_HF_FW + di), pl.BlockSpec((4, HW), fixed)]
                args += [p, p, p, side]
            else:
                in_specs += [cmap(C_GQ), cmap(C_GK), cmap(C_GV), cmap(OFF_LR // 128, 128),
                             pl.BlockSpec((128, HW), fixed), pl.BlockSpec((1, HW), fixed)]
                args += [p, p, p, p, side[di][0], side[di][1]]
            out_specs += [cmap(0), pl.BlockSpec((1, NH, HD, HD), lambda j, chunk=chunk: (chunk(j), 0, 0, 0))]
    return pl.pallas_call(
        body, name=name, grid=(n_chunks,),
        out_shape=[jax.ShapeDtypeStruct((rows, HW), F32),
                   jax.ShapeDtypeStruct((n_chunks, NH, HD, HD), F32)] * (2 * n_br),
        in_specs=in_specs, out_specs=out_specs,
        scratch_shapes=[pltpu.VMEM((2 * n_br, NH, HD, HD), F32)],
        compiler_params=_cparams(dimension_semantics=("arbitrary",)),
    )(*args)


def _scan_bwd_both(p, branch_items, n_ctx_chunks, name):
    rows = p.shape[0]
    n_chunks = rows // CHUNK
    n_ins = [6 if item[0] == "hg" else 8 for item in branch_items]
    n_outs = [4 if item[0] == "hg" else 6 for item in branch_items]
    n_in_all, n_out_all = 2 * sum(n_ins), 2 * sum(n_outs)

    def body(*refs):
        ins, outs, dstate = refs[:n_in_all], refs[n_in_all:n_in_all + n_out_all], refs[-1]
        first = pl.program_id(0) == 0

        @pl.when(first)
        def _():
            dstate[...] = jnp.zeros_like(dstate)

        lanes, where, extra, ctx = [], [], [], []
        ipos = opos = 0
        for bi, item in enumerate(branch_items):
            hg = item[0] == "hg"
            for di, fw in enumerate((True, False)):
                r, w = ins[ipos:ipos + n_ins[bi]], outs[opos:opos + n_outs[bi]]
                ipos += n_ins[bi]
                opos += n_outs[bi]
                if hg:
                    a_ref, b_ref, c_ref, lb_ref, st_ref, do_ref = r
                    acc_refs = (w[3],)
                else:
                    a_ref, b_ref, c_ref, lr_ref, wgk_ref, bgk_ref, st_ref, do_ref = r
                    acc_refs = (w[4], w[5])
                    lr = lr_ref[...]
                    logits = _dot(lr, wgk_ref[...]) + bgk_ref[...]
                    g_all = _log_sigmoid(logits) * (1.0 / GATE_NORM)

                @pl.when(first)
                def _(acc_refs=acc_refs):
                    for ref in acc_refs:
                        ref[...] = jnp.zeros_like(ref)

                for h in range(NH):
                    sl = slice(h * HD, (h + 1) * HD)
                    if hg:
                        hq, hf = a_ref[:, sl], c_ref[:, sl]
                        q, k, g, f, sg, lb = _hg_inputs(hq, hf, lb_ref[...], di, sl)
                        v = b_ref[:, sl]
                        extra.append((hq, f, sg, lb))
                    else:
                        q, k, v, g = a_ref[:, sl] * (HD ** -0.5), b_ref[:, sl], c_ref[:, sl], g_all[:, sl]
                        extra.append(None)
                    lanes.append((q, k, v, g, st_ref[0, h], do_ref[:, sl], dstate[2 * bi + di, h], fw))
                    where.append((2 * bi + di, h, sl))
                ctx.append((hg, w, None if hg else (lr, logits, wgk_ref)))

        dqs, dks, dvs, dgs, dst0s = [], [], [], [], []
        for lo in range(0, len(lanes), BWD_LANES):
            cols = [list(col) for col in zip(*lanes[lo:lo + BWD_LANES])]
            for acc, part in zip((dqs, dks, dvs, dgs, dst0s), _chunks_bwd(*cols)):
                acc.extend(part)
        dg_parts = [[] for _ in ctx]
        for (si, h, sl), ex, dq, dk, dv, dg, dst0 in zip(where, extra, dqs, dks, dvs, dgs, dst0s):
            dstate[si, h] = dst0
            hg, w, _ = ctx[si]
            if hg:
                hq, f, sg, lb = ex
                da_ref, db_ref, dc_ref, dlb_ref = w
                da_ref[:, sl] = (dq * _dsilu(hq)).astype(BF16)
                db_ref[:, sl] = dv.astype(BF16)
                df = dg / f - dk
                dc_ref[:, sl] = (df * (1.0 - lb) * sg * (1.0 - sg)).astype(BF16)
                dlb_ref[0:1, sl] += _colsum(df * (1.0 - sg))
            else:
                da_ref, db_ref, dc_ref = w[:3]
                da_ref[:, sl] = (dq * (HD ** -0.5)).astype(BF16)
                db_ref[:, sl] = dk.astype(BF16)
                dc_ref[:, sl] = dv.astype(BF16)
                dg_parts[si].append(dg)
        for si, (hg, w, more) in enumerate(ctx):
            if not hg:
                dlr_ref, dwgk_ref, dbias_ref = w[3:]
                lr, logits, wgk_ref = more
                dlogits = jnp.concatenate(dg_parts[si], axis=1) * (1.0 / GATE_NORM) * (1.0 - _sigmoid(logits))
                dlr_ref[...] = _dot_nt(dlogits, wgk_ref[...]).astype(BF16)
                dwgk_ref[...] += _dot_tn(lr, dlogits)
                dbias_ref[0:1, :] += _colsum(dlogits)

    fixed = lambda j: (0, 0)
    big = jax.ShapeDtypeStruct((rows, HW), BF16)
    in_specs, args, out_shape, out_specs = [], [], [], []
    for branch, side, states, d_o in branch_items:
        for di, fw in enumerate((True, False)):
            def chunk_of(j, fw=fw):
                return _chunk_index(n_chunks - 1 - j, n_ctx_chunks, n_chunks, fw)

            def cmap(blk, width=HW, chunk_of=chunk_of):
                return pl.BlockSpec((CHUNK, width), lambda j: (chunk_of(j), blk))

            st_spec = pl.BlockSpec((1, NH, HD, HD), lambda j, chunk_of=chunk_of: (chunk_of(j), 0, 0, 0))
            if branch == "hg":
                in_specs += [cmap(C_HQ), cmap(C_HI), cmap(C_HF_FW + di), pl.BlockSpec((4, HW), fixed), st_spec,
                             cmap(0)]
                args += [p, p, p, side, states[di], d_o]
                out_shape += [big, big, big, jax.ShapeDtypeStruct((8, HW), F32)]
                out_specs += [cmap(0), cmap(0), cmap(0), pl.BlockSpec((8, HW), fixed)]
            else:
                in_specs += [cmap(C_GQ), cmap(C_GK), cmap(C_GV), cmap(OFF_LR // 128, 128),
                             pl.BlockSpec((128, HW), fixed), pl.BlockSpec((1, HW), fixed), st_spec, cmap(0)]
                args += [p, p, p, p, side[di][0], side[di][1], states[di], d_o]
                out_shape += [big, big, big, jax.ShapeDtypeStruct((rows, 128), BF16),
                              jax.ShapeDtypeStruct((128, HW), F32), jax.ShapeDtypeStruct((8, HW), F32)]
                out_specs += [cmap(0), cmap(0), cmap(0), cmap(0, 128), pl.BlockSpec((128, HW), fixed),
                              pl.BlockSpec((8, HW), fixed)]
    return pl.pallas_call(
        body, name=name, grid=(n_chunks,), out_shape=out_shape, in_specs=in_specs, out_specs=out_specs,
        scratch_shapes=[pltpu.VMEM((2 * len(branch_items), NH, HD, HD), F32)],
        compiler_params=_cparams(dimension_semantics=("arbitrary",)),
    )(*args)


SMALL_ROWS = 56
ROWS_MOD_X = (0, 1, 8, 16, 17, 18)
ROWS_MOD_C = (2, 3)
ROW_PRE1, ROW_POST1, ROW_ONORM, ROW_PRE2, ROW_POST2, ROW_LB, ROW_BGK, ROW_WGK = 4, 9, 10, 19, 20, 24, 32, 40
ROW_LOSS = 21


def _reduce_small(gathered, lb_full, name, after=()):
    _, _, d = gathered.shape

    def body(g_ref, lb_ref, *rest):
        sum_ref, dmod_ref, dbmod_ref, dlb_ref = rest[len(after):]
        total = g_ref[0]
        for b in range(1, N_DEV):
            total = total + g_ref[b]
        sum_ref[...] = total
        dmod_ref[...] = jnp.zeros_like(dmod_ref)
        for m in range(N_MOD):
            col = slice(m * d, (m + 1) * d)
            acc = jnp.zeros((1, d), F32)
            for b in range(N_DEV):
                row = g_ref[b, ROWS_MOD_X[m]:ROWS_MOD_X[m] + 1, :]
                dmod_ref[b:b + 1, col] = row
                acc = acc + row
            if m < 2:
                ctx_row = total[ROWS_MOD_C[m]:ROWS_MOD_C[m] + 1, :]
                dmod_ref[8:9, col] = ctx_row
                acc = acc + ctx_row
            dbmod_ref[:, col] = acc
        lbv = lb_ref[...]
        for dd in range(2):
            lb = _sigmoid(lbv[dd:dd + 1, :] - lbv[2 + dd:3 + dd, :])
            gl = total[ROW_LB:ROW_LB + 1, dd * HW:(dd + 1) * HW] * lb * (1.0 - lb)
            dlb_ref[dd:dd + 1, :] = gl
            dlb_ref[2 + dd:3 + dd, :] = -gl

    return pl.pallas_call(
        body, name=name,
        out_shape=[jax.ShapeDtypeStruct((SMALL_ROWS, d), F32), jax.ShapeDtypeStruct((16, N_MOD * d), F32),
                   jax.ShapeDtypeStruct((1, N_MOD * d), F32), jax.ShapeDtypeStruct((4, HW), F32)],
        in_specs=[VMEM_SPEC] * 2 + [ANY_SPEC] * len(after), out_specs=[VMEM_SPEC] * 4, compiler_params=_cparams(),
    )(gathered, lb_full, *after)


def _c_ctx_grad(gathered, c_ctx_row, name):
    def body(g_ref, c_ref, o_ref):
        acc = g_ref[0, 0:1, :]
        for chip in range(1, N_CHIP):
            acc = acc + g_ref[2 * chip, 0:1, :]
        o_ref[...] = acc * _dsilu(c_ref[...])

    return pl.pallas_call(
        body, name=name, out_shape=jax.ShapeDtypeStruct(c_ctx_row.shape, F32),
        in_specs=[VMEM_SPEC] * 2, out_specs=VMEM_SPEC, compiler_params=_cparams(),
    )(gathered, c_ctx_row)


def _blocked(full, n_blocks):
    k, n = full.shape
    return full.reshape(k, n_blocks, n // n_blocks).transpose(1, 0, 2)


def _unblocked(blocks):
    nb, k, n = blocks.shape
    return blocks.transpose(1, 0, 2).reshape(k, nb * n)


def _sample_front(x0, ctx0, modc, modx, norm_pre1, lb_full, gla_side, w_in_r):
    ctx_len = ctx0.shape[0]
    n_ctx_tiles = ctx_len // TM
    n_ctx_chunks = ctx_len // CHUNK
    h1, p = _in_projection(ctx0, x0, modc, modx, norm_pre1, w_in_r, n_ctx_tiles, "in_projection")
    (o_hg_fw, st_hg_fw, o_hg_bw, st_hg_bw, o_gla_fw, st_gla_fw, o_gla_bw, st_gla_bw) = _scan_fwd_both(
        p, [("hg", lb_full), ("gla", gla_side)], n_ctx_chunks, "scan_fwd")
    return dict(h1=h1, p=p, o_list=[o_hg_fw, o_hg_bw, o_gla_fw, o_gla_bw],
                states=[st_hg_fw, st_hg_bw, st_gla_fw, st_gla_bw])


def _sample_back(reduce, front, x0, ctx0, target0, modc, modx, norm_pre1, norms, onorms, lb_full, gla_side, w_in_r,
                 wbh, wbg, wout, ffn_weights):
    seq, d = x0.shape
    ctx_len = ctx0.shape[0]
    n_ctx_tiles = ctx_len // TM
    n_tiles = (ctx_len + seq) // TM
    n_ctx_chunks = ctx_len // CHUNK
    h1, p, o_list = front["h1"], front["p"], front["o_list"]
    st_hg_fw, st_hg_bw, st_gla_fw, st_gla_bw = front["states"]
    z2, y1, merged, og_hg, og_gla = _mixer_tail_fwd(x0, p, o_list, modx, norms, onorms, wbh, wbg, wout, n_ctx_tiles,
                                                    "mixer_tail")
    wg, wu, wd = ffn_weights([z2])
    loss_part, dz2, h2, a_act, du, dv, dy2, stat_ffn = _ffn_fwd_bwd(z2, modx, norms, wg, wu, wd, target0, "ffn")
    dff = wg.shape[0]
    tok = reduce("ffn", [_weight_grad(du, h2, "grad_w_ff_gate", tk=dff // 2, tn=d),
                         _weight_grad(dv, h2, "grad_w_ff_up", tk=dff // 2, tn=d),
                         _weight_grad(a_act, dy2, "grad_w_ff_down", tk=dff // 2)])

    (d_ohg, d_ogla, d_hgate, d_ggate, d_ghg, d_ggla, dy1, db_hg, db_gla, stat_mix) = _mixer_tail_bwd(
        x0, p, o_list, dz2, y1, modx + tok, norms, onorms, wbh, wbg, wout, n_ctx_tiles, n_tiles, "mixer_tail_bwd")
    tok = reduce("mix", [_weight_grad(og_hg, db_hg, "grad_w_br_hg"), _weight_grad(og_gla, db_gla, "grad_w_br_gla"),
                         _weight_grad(merged, dy1, "grad_w_out")])
    tok = tok + reduce("push_ffn", [dy1])
    gla_b = [(wgk, bias + tok) for wgk, bias in gla_side]
    (dgq_f, dgk_f, dgv_f, dlr_f, dwgk_f, dbgk_f, dgq_b, dgk_b, dgv_b, dlr_b, dwgk_b, dbgk_b,
     dhq_f, dhi_f, dhf_f, dlb_f, dhq_b, dhi_b, dhf_b, dlb_b) = _scan_bwd_both(
        p, [("gla", gla_b, (st_gla_fw, st_gla_bw), d_ogla), ("hg", lb_full, (st_hg_fw, st_hg_bw), d_ohg)],
        n_ctx_chunks, "scan_bwd")
    tok = reduce("push_mix", [dbgk_f])
    pieces = [dhq_f, dhq_b, dhi_f, dhi_b, dhf_f, dhf_b, d_hgate, dgq_f, dgq_b, dgk_f, dgk_b, dgv_f, dgv_b, d_ggate,
              d_ghg, d_ggla, dlr_f, dlr_b]
    dp, grad_x, stat_in = _in_projection_bwd(ctx0, x0, dz2, modc, modx, norm_pre1 + tok, w_in_r, pieces, n_ctx_tiles,
                                             "in_projection_bwd")

    started = reduce("small_start", dict(stat_in=stat_in, stat_mix=stat_mix, stat_ffn=stat_ffn, dlb=(dlb_f, dlb_b),
                                         dwgk=(dwgk_f, dwgk_b), dbgk=(dbgk_f, dbgk_b)))
    reduce("in", [_w_in_grad(dp, h1, w_in_r.shape[0], "grad_w_in", after=[started])])
    reduce("small", None)
    reduce("push_in", [])
    return dict(loss_part=loss_part, grad_x=grad_x)


def kernel(x, c, ctx, c_ctx, w_mod, b_mod, norm_pre1, norm_post1, norm_pre2, norm_post2, w_in, hg_lb, hg_onorm, gla_w_gk, gla_b_gk, gla_onorm, w_br_hg, w_br_gla, w_out, w_ff_gate, w_ff_up, w_ff_down, loss_target, m_c_ctx, m_w_mod, m_b_mod, m_norm_pre1, m_norm_post1, m_norm_pre2, m_norm_post2, m_w_in, m_hg_lb, m_hg_onorm, m_gla_w_gk, m_gla_b_gk, m_gla_onorm, m_w_br_hg, m_w_br_gla, m_w_out, m_w_ff_gate, m_w_ff_up, m_w_ff_down, v_c_ctx, v_w_mod, v_b_mod, v_norm_pre1, v_norm_post1, v_norm_pre2, v_norm_post2, v_w_in, v_hg_lb, v_hg_onorm, v_gla_w_gk, v_gla_b_gk, v_gla_onorm, v_w_br_hg, v_w_br_gla, v_w_out, v_w_ff_gate, v_w_ff_up, v_w_ff_down):
    seq, d = x.shape[1], x.shape[2]
    ctx_len = ctx.shape[1]
    assert seq % TM == 0 and ctx_len % TM == 0 and d == 2 * HW
    ax, ay, ac = lax.axis_index("x"), lax.axis_index("y"), lax.axis_index("c")
    chip = 2 * ax + ay
    dev = 2 * chip + ac
    c_arr = jnp.reshape(ac, (1,)).astype(jnp.int32)
    chip_arr = jnp.reshape(chip, (1,)).astype(jnp.int32)
    transposed = ("w_in", "w_ff_gate", "w_ff_up")
    view = lambda a, nm: a[0].T if nm in transposed else a[0]

    sems_in, lands_in, token_in0 = _blocks_start([_cast_into_blocks(chip_arr, view(w_in, "w_in"), "cast_w_in")],
                                                 "gather_w_in_start")

    nc = d // 128
    pad8 = lambda a: jnp.pad(a, ((0, -a.shape[0] % 8), (0, 0)))
    small1 = jnp.concatenate([c.reshape(nc, 128) + token_in0[0, 0], pad8(hg_lb.reshape(4, 128)),
                              gla_w_gk.reshape(2 * RANK, 128), pad8(gla_b_gk.reshape(2, 128))], axis=0)
    blocks = [_cast_into_blocks(chip_arr, view(w_, nm), "cast_" + nm) for w_, nm in (
        (w_br_hg, "w_br_hg"), (w_br_gla, "w_br_gla"), (w_out, "w_out"), (w_ff_gate, "w_ff_gate"),
        (w_ff_up, "w_ff_up"), (w_ff_down, "w_ff_down"))]
    got1 = _allgather8(small1, "gather_small_params", after=blocks)
    c_all = got1[:, :nc, :].reshape(N_DEV, d)
    per_chip = got1[0::2]
    lb_full = per_chip[:, nc:nc + 4, :].transpose(1, 0, 2).reshape(4, HW)
    wgk_full = per_chip[:, nc + 8:nc + 8 + 2 * RANK, :].transpose(1, 0, 2).reshape(2, RANK, HW)
    bgk_full = per_chip[:, nc + 8 + 2 * RANK:nc + 10 + 2 * RANK, :].transpose(1, 0, 2).reshape(2, HW)
    wgk_pad = [jnp.zeros((128, HW), F32).at[dd * RANK:(dd + 1) * RANK].set(wgk_full[dd]) for dd in range(2)]
    bgk = [bgk_full[dd:dd + 1] for dd in range(2)]

    n_mod_cols = w_mod.shape[2]
    cond = jnp.concatenate([c_all, pad8(c_ctx.reshape(1, d))], axis=0)
    b_cols = lax.dynamic_slice(b_mod, (0, chip * n_mod_cols), (1, n_mod_cols))
    lands_in = _blocks_wait(sems_in, lands_in, [got1], "gather_w_in_wait")
    fwd_sems, lands_in, fwd_token = _forward_start(lands_in, "gather_w_in_forward_start")
    mod_part = _mod_forward(cond + fwd_token[0, 0], w_mod[0], b_cols, "mod_forward")
    mod_got = _allgather8(mod_part, "gather_mod")
    mod_all = mod_got[0::2].transpose(1, 0, 2).reshape(16, N_CHIP * n_mod_cols)
    modx = pad8(lax.dynamic_slice(mod_all, (dev, 0), (1, N_MOD * d)).reshape(N_MOD, d))
    modc = pad8(mod_all[8].reshape(N_MOD, d))

    gathered_in = _forward_wait(fwd_sems, lands_in, [mod_got], "gather_w_in_forward_wait")
    sems, lands, token = _blocks_start(blocks, "gather_rest_start", after=[gathered_in[0]])
    w_in_r = gathered_in[0].reshape(-1, d)

    norms = jnp.concatenate([norm_pre1, norm_post1, norm_pre2, norm_post2, jnp.zeros((4, d), F32)], axis=0)
    onorms = jnp.zeros((8, d), F32).at[0, :HD].set(hg_onorm[0]).at[1, :HD].set(gla_onorm[0])
    gla_side = [(wgk_pad[dd], bgk[dd]) for dd in range(2)]
    modx = modx + token[0, 0]
    front = _sample_front(x[0], ctx[0], modc, modx, norm_pre1, lb_full, gla_side, w_in_r)
    lands = _blocks_wait(sems, lands, front["o_list"], "gather_rest_wait")
    gathered = _blocks_finish(lands[:3], "gather_mix_finish")
    wbh, wbg = _unblocked(gathered[0]), _unblocked(gathered[1])
    wout = gathered[2].reshape(d, d)
    ffn_sems, ffn_lands, ffn_token = _forward_start(lands[3:], "gather_ffn_forward_start")
    onorms = onorms + ffn_token[0, 0]

    def ffn_weights(after):
        got = _forward_wait(ffn_sems, ffn_lands, after, "gather_ffn_forward_wait")
        return tuple(g.reshape(-1, d) for g in got)

    dff = w_ff_down.shape[1] * N_CHIP
    groups = {"ffn": ["w_ff_gate", "w_ff_up", "w_ff_down"], "mix": ["w_br_hg", "w_br_gla", "w_out"], "in": ["w_in"]}
    row_sharded = {"w_out": d // N_CHIP, "w_ff_down": dff // N_CHIP, "w_ff_gate": dff // N_CHIP,
                   "w_ff_up": dff // N_CHIP, "w_in": w_in.shape[2]}
    in_flight, to_sibling, small = {}, {}, {}

    def reduce_small_start(stats):
        small2 = jnp.concatenate([
            stats["stat_in"], stats["stat_mix"], stats["stat_ffn"],
            jnp.concatenate(stats["dlb"], axis=1), jnp.concatenate(stats["dbgk"], axis=1),
            jnp.concatenate([stats["dwgk"][0][0:RANK], stats["dwgk"][1][RANK:2 * RANK]], axis=1)], axis=0)
        assert small2.shape[0] == SMALL_ROWS
        sems_, land_, token_ = _allgather8_start(small2, "gather_small_grads_start")
        small.update(gathering=(sems_, land_))
        return token_

    def reduce_small():
        got2 = small["gathered"]
        total, dmod_all, g_b_mod, g_lb_full = _reduce_small(got2, lb_full, "reduce_small", to_sibling["in"][1])
        dmod_cols = lax.dynamic_slice(dmod_all, (0, chip * n_mod_cols), (16, n_mod_cols))
        g_w_mod, cctx_part = _mod_backward(cond, w_mod[0], dmod_cols, "mod_backward")
        got3 = _allgather8(cctx_part, "gather_c_ctx_grad")
        g_c_ctx = _c_ctx_grad(got3, c_ctx.reshape(1, d), "c_ctx_grad")
        small.update(total=total, g_b_mod=g_b_mod, g_lb_full=g_lb_full, g_w_mod=g_w_mod, g_c_ctx=g_c_ctx)

    def reduce(group, grads):
        if group == "small_start":
            return reduce_small_start(grads)
        if group == "small":
            return reduce_small()
        if group.startswith("push_"):
            return push(group[5:], grads)
        nms = groups[group]
        after = []
        if group == "in":
            small.update(gathered=_allgather8_wait(*small["gathering"], grads, "gather_small_grads_wait"))
            after = [small["gathered"]]
        full = [g.reshape(N_CHIP, row_sharded[nm], d) if nm in row_sharded else _blocked(g, N_CHIP)
                for g, nm in zip(grads, nms)]
        sems_, full, lands_, token_ = _send_half_start(full, "grads_to_sibling_start_" + group, after)
        to_sibling[group] = (sems_, full, lands_)
        return token_[0, 0]

    def push(group, after):
        nms = groups[group]
        sems_, full, lands_ = to_sibling[group]
        if group == "in":
            after = list(after) + [small["g_c_ctx"], small["total"]]
        full, from_sibling = _send_half_wait(sems_, full, lands_, after, "grads_to_sibling_wait_" + group)
        pairs = [_pair_sum(c_arr, f, r_, "pair_sum_" + nm) for f, r_, nm in zip(full, from_sibling, nms)]
        after = [small["g_c_ctx"], small["total"]] if group == "in" else []
        sems_, pairs, lands_, token_ = _scatter_start(pairs, "grads_to_owner_start_" + group, after)
        in_flight[group] = (sems_, pairs, lands_, token_)
        return token_[0, 0]

    r = _sample_back(reduce, front, x[0], ctx[0], loss_target[0], modc, modx, norm_pre1, norms, onorms, lb_full,
                     gla_side, w_in_r, wbh, wbg, wout, ffn_weights)
    grad_x = r["grad_x"]

    weights = dict(w_in=(w_in, m_w_in, v_w_in), w_br_hg=(w_br_hg, m_w_br_hg, v_w_br_hg),
                   w_br_gla=(w_br_gla, m_w_br_gla, v_w_br_gla), w_out=(w_out, m_w_out, v_w_out),
                   w_ff_gate=(w_ff_gate, m_w_ff_gate, v_w_ff_gate), w_ff_up=(w_ff_up, m_w_ff_up, v_w_ff_up),
                   w_ff_down=(w_ff_down, m_w_ff_down, v_w_ff_down))
    names = ["w_in", "w_br_hg", "w_br_gla", "w_out", "w_ff_gate", "w_ff_up", "w_ff_down"]
    big, swapping = {}, {}

    def sum_and_swap(group, after):
        sems_, pairs, lands_, _ = in_flight[group]
        pairs, lands_ = _scatter_wait(sems_, pairs, lands_, after, "grads_to_owner_wait_" + group)
        own_half = [_sum_owner(chip_arr, pr, g, "chip_sum_" + nm) for pr, g, nm in zip(pairs, lands_, groups[group])]
        swapping[group] = _swap_start(own_half, "halves_to_sibling_start_" + group)
        return own_half[-1]

    def update(group, after):
        sems_, own_half, lands_ = swapping[group]
        own_half, other_half = _swap_wait(sems_, own_half, lands_, after, "halves_to_sibling_wait_" + group)
        done = []
        for nm, own, oth in zip(groups[group], own_half, other_half):
            w_, m_, v_ = (view(a, nm) for a in weights[nm])
            res = _adamw_halves(c_arr, own, oth, w_, m_, v_, "adamw_" + nm)
            big[nm] = [r_.T[None] if nm in transposed else r_[None] for r_ in res]
            done.append(res[1])
        return done

    token_in = in_flight["in"][3]
    summed_ffn = sum_and_swap("ffn", [token_in])
    summed_mix = sum_and_swap("mix", [summed_ffn])

    total, g_b_mod, g_lb_full, g_w_mod, g_c_ctx = (small[k] for k in ("total", "g_b_mod", "g_lb_full", "g_w_mod",
                                                                      "g_c_ctx"))
    g_pre1, g_post1, g_pre2, g_post2 = (total[r_:r_ + 1] for r_ in (ROW_PRE1, ROW_POST1, ROW_PRE2, ROW_POST2))
    g_hg_on, g_gla_on = total[ROW_ONORM:ROW_ONORM + 1, 0:HD], total[ROW_ONORM:ROW_ONORM + 1, HD:2 * HD]
    n_lb = hg_lb.shape[2]
    g_hg_lb = lax.dynamic_slice(g_lb_full, (0, chip * n_lb), (4, n_lb))
    g_bgk = lax.dynamic_slice(total[ROW_BGK:ROW_BGK + 1].reshape(2, HW), (0, chip * n_lb), (2, n_lb))
    g_wgk_full = total[ROW_WGK:ROW_WGK + RANK].reshape(RANK, 2, HW).transpose(1, 0, 2).reshape(2 * RANK, HW)
    g_wgk = lax.dynamic_slice(g_wgk_full, (0, chip * n_lb), (2 * RANK, n_lb))

    small_items = [
        (g_c_ctx, c_ctx.reshape(1, d), m_c_ctx.reshape(1, d), v_c_ctx.reshape(1, d)),
        (g_b_mod, b_mod, m_b_mod, v_b_mod),
        (g_pre1, norm_pre1, m_norm_pre1, v_norm_pre1),
        (g_post1, norm_post1, m_norm_post1, v_norm_post1),
        (g_pre2, norm_pre2, m_norm_pre2, v_norm_pre2),
        (g_post2, norm_post2, m_norm_post2, v_norm_post2),
        (g_hg_lb, hg_lb.reshape(4, n_lb), m_hg_lb.reshape(4, n_lb), v_hg_lb.reshape(4, n_lb)),
        (g_hg_on, hg_onorm, m_hg_onorm, v_hg_onorm),
        (g_wgk, gla_w_gk.reshape(2 * RANK, n_lb), m_gla_w_gk.reshape(2 * RANK, n_lb), v_gla_w_gk.reshape(2 * RANK, n_lb)),
        (g_bgk, gla_b_gk.reshape(2, n_lb), m_gla_b_gk.reshape(2, n_lb), v_gla_b_gk.reshape(2, n_lb)),
        (g_gla_on, gla_onorm, m_gla_onorm, v_gla_onorm),
    ]
    small_res = _adamw_whole(small_items, "adamw_small")
    mod_res = _adamw_tiled(g_w_mod, w_mod[0], m_w_mod[0], v_w_mod[0], "adamw_w_mod")
    done_ffn = update("ffn", [summed_mix, mod_res[0], small_res[0][0]])
    done_mix = update("mix", done_ffn)
    update("in", [sum_and_swap("in", done_mix)])

    loss = total[ROW_LOSS, 0]

    shapes = dict(c_ctx=c_ctx.shape, b_mod=b_mod.shape, norm_pre1=norm_pre1.shape, norm_post1=norm_post1.shape,
                  norm_pre2=norm_pre2.shape, norm_post2=norm_post2.shape, hg_lb=hg_lb.shape, hg_onorm=hg_onorm.shape,
                  gla_w_gk=gla_w_gk.shape, gla_b_gk=gla_b_gk.shape, gla_onorm=gla_onorm.shape)
    small_names = ["c_ctx", "b_mod", "norm_pre1", "norm_post1", "norm_pre2", "norm_post2", "hg_lb", "hg_onorm",
                   "gla_w_gk", "gla_b_gk", "gla_onorm"]
    grads, deltas, new_m, new_v = {}, {}, {}, {}
    for nm, item, res in zip(small_names, small_items, small_res):
        grads[nm] = item[0].reshape(shapes[nm])
        deltas[nm], new_m[nm], new_v[nm] = (r_.reshape(shapes[nm]) for r_ in res)
    grads["w_mod"] = g_w_mod[None]
    deltas["w_mod"], new_m["w_mod"], new_v["w_mod"] = (r_[None] for r_ in mod_res)
    for nm in names:
        grads[nm], deltas[nm], new_m[nm], new_v[nm] = big[nm]
    order = ["c_ctx", "w_mod", "b_mod", "norm_pre1", "norm_post1", "norm_pre2", "norm_post2", "w_in", "hg_lb",
             "hg_onorm", "gla_w_gk", "gla_b_gk", "gla_onorm", "w_br_hg", "w_br_gla", "w_out", "w_ff_gate", "w_ff_up",
             "w_ff_down"]
    return (loss, grad_x[None], *[grads[n] for n in order], *[deltas[n] for n in order],
            *[new_m[n] for n in order], *[new_v[n] for n in order])
```
